```python
import math
import jax, jax.numpy as jnp
from jax import lax
import numpy as np

D_MODEL = 1024
BATCH = 8
SEQ = 2048
DEPTH = 1

DILATED_PATTERNS = ((128, 1), (512, 4), (2048, 16))
N_GROUPS = len(DILATED_PATTERNS)
HEADS_PER_GROUP = 4
N_HEADS_A = N_GROUPS * HEADS_PER_GROUP
HEAD_DIM = 128
QKV_WIDTH = N_HEADS_A * HEAD_DIM
ATTN_OUT = HEADS_PER_GROUP * HEAD_DIM
CONV_DIM = D_MODEL
CONV_WIDTH = 31
D_FF = 2816
FFN_CONV_WIDTH = 3
N_BUCKETS = 32
MAX_DISTANCE = 2048
IN_WIDTH = 3 * QKV_WIDTH + 2 * CONV_DIM + 2 * D_MODEL
RMS_EPS = 1e-6
LN_EPS = 1e-5
NEG_INF = -1e30

kernel_name = "hybrid_dilated_attn_conformer_convffn"


def rms_norm(x, g):
    xf = x.astype(jnp.float32)
    y = xf * lax.rsqrt(jnp.mean(xf * xf, axis=-1, keepdims=True) + RMS_EPS)
    return (y * g.astype(jnp.float32)).astype(x.dtype)


def layer_norm(x, g, b):
    xf = x.astype(jnp.float32)
    mu = jnp.mean(xf, axis=-1, keepdims=True)
    xc = xf - mu
    var = jnp.mean(xc * xc, axis=-1, keepdims=True)
    y = xc * lax.rsqrt(var + LN_EPS) * g.astype(jnp.float32) + b.astype(jnp.float32)
    return y.astype(x.dtype)


def causal_depthwise_conv(x, w, b):
    K, C = w.shape
    y = lax.conv_general_dilated(
        x, w[:, None, :].astype(x.dtype), window_strides=(1,), padding=[(K - 1, 0)],
        dimension_numbers=("NWC", "WIO", "NWC"), feature_group_count=C)
    return y + b.astype(x.dtype)


def t5_bucket(dist):
    max_exact = N_BUCKETS // 2
    is_small = dist < max_exact
    d = jnp.maximum(dist, 1).astype(jnp.float32)
    large = max_exact + (jnp.log(d / max_exact) / math.log(MAX_DISTANCE / max_exact)
                         * (N_BUCKETS - max_exact)).astype(jnp.int32)
    large = jnp.minimum(large, N_BUCKETS - 1)
    return jnp.where(is_small, dist, large)


def dilated_group_attention(q, k, v, bias_g, window, dilation):
    B, S, H, Dh = q.shape
    r = dilation
    span = window // dilation
    L = S // r
    nb = -(-L // span)
    Lp = nb * span

    def split(t):
        t = t.reshape(B, L, r, H, Dh).transpose(0, 2, 1, 3, 4)
        t = jnp.pad(t, ((0, 0), (0, 0), (0, Lp - L), (0, 0), (0, 0)))
        return t.reshape(B, r, nb, span, H, Dh)

    def band(t):
        prev = jnp.pad(t, ((0, 0), (0, 0), (1, 0), (0, 0), (0, 0), (0, 0)))[:, :, :-1]
        return jnp.concatenate([prev, t], axis=3)

    qb = split(q)
    kk = band(split(k))
    vv = band(split(v))

    s = jnp.einsum("brnqhd,brnkhd->brnhqk", qb, kk).astype(jnp.float32) * (Dh ** -0.5)

    qi = jnp.arange(span)[:, None]
    ki = jnp.arange(2 * span)[None, :]
    rel = qi + span - ki
    blk = jnp.arange(nb)[:, None, None]
    valid = (rel >= 0) & (rel <= span) & (blk * span + ki - span >= 0)
    bucket = t5_bucket(jnp.maximum(rel, 0) * r)
    bias = bias_g[bucket].astype(jnp.float32).transpose(2, 0, 1)

    s = jnp.where(valid[None, None, :, None], s + bias, NEG_INF)
    m = jnp.max(s, axis=-1)
    p = jnp.exp(s - m[..., None])
    l = jnp.sum(p, axis=-1)
    o = jnp.einsum("brnhqk,brnkhd->brnqhd", p, vv.astype(jnp.float32))
    m = jnp.swapaxes(m, -1, -2)
    l = jnp.swapaxes(l, -1, -2)
    o = o / l[..., None]

    def merge(t):
        t = t.reshape((B, r, Lp) + t.shape[4:])[:, :, :L]
        return jnp.swapaxes(t, 1, 2).reshape((B, S) + t.shape[3:])

    return merge(o), merge(m), merge(l)


def dilated_attention(q, k, v, rel_bias):
    outs, maxes, dens = [], [], []
    for g, (window, dilation) in enumerate(DILATED_PATTERNS):
        hs = slice(g * HEADS_PER_GROUP, (g + 1) * HEADS_PER_GROUP)
        o, m, l = dilated_group_attention(q[:, :, hs], k[:, :, hs], v[:, :, hs],
                                          rel_bias[:, hs], window, dilation)
        outs.append(o); maxes.append(m); dens.append(l)
    o = jnp.stack(outs)
    m = jnp.stack(maxes)
    l = jnp.stack(dens)
    w = l * jnp.exp(m - jnp.max(m, axis=0, keepdims=True))
    return jnp.sum(w[..., None] * o, axis=0) / jnp.sum(w, axis=0)[..., None]


def _fwd_setup_inputs(seed: int = 0) -> dict:
    key = jax.random.key(seed)
    ks = jax.random.split(key, 24)
    f32 = jnp.float32
    nrm = lambda k, shape, scale: jax.random.normal(k, shape, f32) * scale
    gain = lambda k, shape: 1.0 + 0.05 * jax.random.normal(k, shape, f32)
    return {
        "x": jax.random.normal(ks[0], (BATCH, SEQ, D_MODEL), f32),
        "w_in": nrm(ks[1], (DEPTH, D_MODEL, IN_WIDTH), D_MODEL ** -0.5),
        "b_gate": nrm(ks[2], (DEPTH, 2 * D_MODEL), 0.02),
        "rel_bias": nrm(ks[3], (N_BUCKETS, N_HEADS_A), 0.5),
        "w_attn_out": nrm(ks[4], (DEPTH, ATTN_OUT, D_MODEL), ATTN_OUT ** -0.5),
        "conv_dw_w": nrm(ks[5], (DEPTH, CONV_WIDTH, CONV_DIM), CONV_WIDTH ** -0.5),
        "conv_dw_b": nrm(ks[6], (DEPTH, CONV_DIM), 0.02),
        "conv_ln_g": gain(ks[7], (DEPTH, CONV_DIM)),
        "conv_ln_b": nrm(ks[8], (DEPTH, CONV_DIM), 0.02),
        "conv_pw_w": nrm(ks[9], (DEPTH, CONV_DIM, D_MODEL), CONV_DIM ** -0.5),
        "w_out": nrm(ks[10], (DEPTH, D_MODEL, D_MODEL), D_MODEL ** -0.5),
        "norm_mix_pre": gain(ks[11], (DEPTH, D_MODEL)),
        "norm_mix_post": gain(ks[12], (DEPTH, D_MODEL)),
        "norm_ffn_pre": gain(ks[13], (DEPTH, D_MODEL)),
        "norm_ffn_post": gain(ks[14], (DEPTH, D_MODEL)),
        "w_up": nrm(ks[15], (DEPTH, D_MODEL, 2 * D_FF), D_MODEL ** -0.5),
        "ffn_conv_w": nrm(ks[16], (DEPTH, FFN_CONV_WIDTH, 2 * D_FF), FFN_CONV_WIDTH ** -0.5),
        "ffn_conv_b": nrm(ks[17], (DEPTH, 2 * D_FF), 0.02),
        "w_down": nrm(ks[18], (DEPTH, D_FF, D_MODEL), D_FF ** -0.5),
    }


def _fwd_reference(x, w_in, b_gate, rel_bias, w_attn_out, conv_dw_w, conv_dw_b, conv_ln_g,
              conv_ln_b, conv_pw_w, w_out, norm_mix_pre, norm_mix_post, norm_ffn_pre,
              norm_ffn_post, w_up, ffn_conv_w, ffn_conv_b, w_down):
    B, S, D = x.shape
    for layer in range(DEPTH):
        h = rms_norm(x, norm_mix_pre[layer])
        proj = h @ w_in[layer].astype(h.dtype)
        q, k, v, glu_in, gates = jnp.split(
            proj, np.cumsum([QKV_WIDTH, QKV_WIDTH, QKV_WIDTH, 2 * CONV_DIM]).tolist(), axis=-1)
        q = q.reshape(B, S, N_HEADS_A, HEAD_DIM)
        k = k.reshape(B, S, N_HEADS_A, HEAD_DIM)
        v = v.reshape(B, S, N_HEADS_A, HEAD_DIM)

        a = dilated_attention(q, k, v, rel_bias).reshape(B, S, ATTN_OUT).astype(x.dtype)
        y_a = a @ w_attn_out[layer].astype(x.dtype)

        c_val, c_gate = jnp.split(glu_in, 2, axis=-1)
        c = c_val * jax.nn.sigmoid(c_gate)
        c = causal_depthwise_conv(c, conv_dw_w[layer], conv_dw_b[layer])
        c = jax.nn.silu(layer_norm(c, conv_ln_g[layer], conv_ln_b[layer]))
        y_c = c @ conv_pw_w[layer].astype(x.dtype)

        g = jax.nn.sigmoid(gates + b_gate[layer].astype(gates.dtype))
        g_a, g_c = jnp.split(g, 2, axis=-1)
        mixed = g_a * y_a + g_c * y_c
        out = mixed @ w_out[layer].astype(x.dtype)
        x = x + rms_norm(out, norm_mix_post[layer])

        h = rms_norm(x, norm_ffn_pre[layer])
        u = h @ w_up[layer].astype(h.dtype)
        u = causal_depthwise_conv(u, ffn_conv_w[layer], ffn_conv_b[layer])
        u_gate, u_val = jnp.split(u, 2, axis=-1)
        f = jax.nn.gelu(u_gate, approximate=True) * u_val
        y = f @ w_down[layer].astype(f.dtype)
        x = x + rms_norm(y, norm_ffn_post[layer])
    return x


import jax as _jax
import jax.numpy as _jnp

TWIN_FORMAT = 'train_step'
FWD_PARAMS = ['x', 'w_in', 'b_gate', 'rel_bias', 'w_attn_out', 'conv_dw_w', 'conv_dw_b', 'conv_ln_g', 'conv_ln_b', 'conv_pw_w', 'w_out', 'norm_mix_pre', 'norm_mix_post', 'norm_ffn_pre', 'norm_ffn_post', 'w_up', 'ffn_conv_w', 'ffn_conv_b', 'w_down']
TWIN_WEIGHTS = ['w_in', 'b_gate', 'rel_bias', 'w_attn_out', 'conv_dw_w', 'conv_dw_b', 'conv_ln_g', 'conv_ln_b', 'conv_pw_w', 'w_out', 'norm_mix_pre', 'norm_mix_post', 'norm_ffn_pre', 'norm_ffn_post', 'w_up', 'ffn_conv_w', 'ffn_conv_b', 'w_down']
TWIN_DIFF_INPUT = 'x'
TWIN_INPUTS = ['x', 'w_in', 'b_gate', 'rel_bias', 'w_attn_out', 'conv_dw_w', 'conv_dw_b', 'conv_ln_g', 'conv_ln_b', 'conv_pw_w', 'w_out', 'norm_mix_pre', 'norm_mix_post', 'norm_ffn_pre', 'norm_ffn_post', 'w_up', 'ffn_conv_w', 'ffn_conv_b', 'w_down', 'loss_target', 'm_w_in', 'm_b_gate', 'm_rel_bias', 'm_w_attn_out', 'm_conv_dw_w', 'm_conv_dw_b', 'm_conv_ln_g', 'm_conv_ln_b', 'm_conv_pw_w', 'm_w_out', 'm_norm_mix_pre', 'm_norm_mix_post', 'm_norm_ffn_pre', 'm_norm_ffn_post', 'm_w_up', 'm_ffn_conv_w', 'm_ffn_conv_b', 'm_w_down', 'v_w_in', 'v_b_gate', 'v_rel_bias', 'v_w_attn_out', 'v_conv_dw_w', 'v_conv_dw_b', 'v_conv_ln_g', 'v_conv_ln_b', 'v_conv_pw_w', 'v_w_out', 'v_norm_mix_pre', 'v_norm_mix_post', 'v_norm_ffn_pre', 'v_norm_ffn_post', 'v_w_up', 'v_ffn_conv_w', 'v_ffn_conv_b', 'v_w_down']
TWIN_OUTPUTS = ['loss', 'grad_x', 'grad_w_in', 'grad_b_gate', 'grad_rel_bias', 'grad_w_attn_out', 'grad_conv_dw_w', 'grad_conv_dw_b', 'grad_conv_ln_g', 'grad_conv_ln_b', 'grad_conv_pw_w', 'grad_w_out', 'grad_norm_mix_pre', 'grad_norm_mix_post', 'grad_norm_ffn_pre', 'grad_norm_ffn_post', 'grad_w_up', 'grad_ffn_conv_w', 'grad_ffn_conv_b', 'grad_w_down', 'delta_w_in', 'delta_b_gate', 'delta_rel_bias', 'delta_w_attn_out', 'delta_conv_dw_w', 'delta_conv_dw_b', 'delta_conv_ln_g', 'delta_conv_ln_b', 'delta_conv_pw_w', 'delta_w_out', 'delta_norm_mix_pre', 'delta_norm_mix_post', 'delta_norm_ffn_pre', 'delta_norm_ffn_post', 'delta_w_up', 'delta_ffn_conv_w', 'delta_ffn_conv_b', 'delta_w_down', 'new_m_w_in', 'new_m_b_gate', 'new_m_rel_bias', 'new_m_w_attn_out', 'new_m_conv_dw_w', 'new_m_conv_dw_b', 'new_m_conv_ln_g', 'new_m_conv_ln_b', 'new_m_conv_pw_w', 'new_m_w_out', 'new_m_norm_mix_pre', 'new_m_norm_mix_post', 'new_m_norm_ffn_pre', 'new_m_norm_ffn_post', 'new_m_w_up', 'new_m_ffn_conv_w', 'new_m_ffn_conv_b', 'new_m_w_down', 'new_v_w_in', 'new_v_b_gate', 'new_v_rel_bias', 'new_v_w_attn_out', 'new_v_conv_dw_w', 'new_v_conv_dw_b', 'new_v_conv_ln_g', 'new_v_conv_ln_b', 'new_v_conv_pw_w', 'new_v_w_out', 'new_v_norm_mix_pre', 'new_v_norm_mix_post', 'new_v_norm_ffn_pre', 'new_v_norm_ffn_post', 'new_v_w_up', 'new_v_ffn_conv_w', 'new_v_ffn_conv_b', 'new_v_w_down']
TWIN_LEAF_KINDS = {'loss': 'loss', 'grad_x': 'grad_x', 'grad_w_in': 'grad_w', 'grad_b_gate': 'grad_w', 'grad_rel_bias': 'grad_w', 'grad_w_attn_out': 'grad_w', 'grad_conv_dw_w': 'grad_w', 'grad_conv_dw_b': 'grad_w', 'grad_conv_ln_g': 'grad_w', 'grad_conv_ln_b': 'grad_w', 'grad_conv_pw_w': 'grad_w', 'grad_w_out': 'grad_w', 'grad_norm_mix_pre': 'grad_w', 'grad_norm_mix_post': 'grad_w', 'grad_norm_ffn_pre': 'grad_w', 'grad_norm_ffn_post': 'grad_w', 'grad_w_up': 'grad_w', 'grad_ffn_conv_w': 'grad_w', 'grad_ffn_conv_b': 'grad_w', 'grad_w_down': 'grad_w', 'delta_w_in': 'delta_w', 'delta_b_gate': 'delta_w', 'delta_rel_bias': 'delta_w', 'delta_w_attn_out': 'delta_w', 'delta_conv_dw_w': 'delta_w', 'delta_conv_dw_b': 'delta_w', 'delta_conv_ln_g': 'delta_w', 'delta_conv_ln_b': 'delta_w', 'delta_conv_pw_w': 'delta_w', 'delta_w_out': 'delta_w', 'delta_norm_mix_pre': 'delta_w', 'delta_norm_mix_post': 'delta_w', 'delta_norm_ffn_pre': 'delta_w', 'delta_norm_ffn_post': 'delta_w', 'delta_w_up': 'delta_w', 'delta_ffn_conv_w': 'delta_w', 'delta_ffn_conv_b': 'delta_w', 'delta_w_down': 'delta_w', 'new_m_w_in': 'new_m', 'new_m_b_gate': 'new_m', 'new_m_rel_bias': 'new_m', 'new_m_w_attn_out': 'new_m', 'new_m_conv_dw_w': 'new_m', 'new_m_conv_dw_b': 'new_m', 'new_m_conv_ln_g': 'new_m', 'new_m_conv_ln_b': 'new_m', 'new_m_conv_pw_w': 'new_m', 'new_m_w_out': 'new_m', 'new_m_norm_mix_pre': 'new_m', 'new_m_norm_mix_post': 'new_m', 'new_m_norm_ffn_pre': 'new_m', 'new_m_norm_ffn_post': 'new_m', 'new_m_w_up': 'new_m', 'new_m_ffn_conv_w': 'new_m', 'new_m_ffn_conv_b': 'new_m', 'new_m_w_down': 'new_m', 'new_v_w_in': 'new_v', 'new_v_b_gate': 'new_v', 'new_v_rel_bias': 'new_v', 'new_v_w_attn_out': 'new_v', 'new_v_conv_dw_w': 'new_v', 'new_v_conv_dw_b': 'new_v', 'new_v_conv_ln_g': 'new_v', 'new_v_conv_ln_b': 'new_v', 'new_v_conv_pw_w': 'new_v', 'new_v_w_out': 'new_v', 'new_v_norm_mix_pre': 'new_v', 'new_v_norm_mix_post': 'new_v', 'new_v_norm_ffn_pre': 'new_v', 'new_v_norm_ffn_post': 'new_v', 'new_v_w_up': 'new_v', 'new_v_ffn_conv_w': 'new_v', 'new_v_ffn_conv_b': 'new_v', 'new_v_w_down': 'new_v'}


def _forward(args):
    return _fwd_reference(*[args[k] for k in FWD_PARAMS])


def _output_shape():
    out = _jax.eval_shape(lambda: _forward(_fwd_setup_inputs(0)))
    return out.shape, out.dtype

N_MICROBATCH = 1
ADAM_LR = 0.001
ADAM_B1 = 0.9
ADAM_B2 = 0.999
ADAM_EPS = 1e-08
ADAM_WD = 0.01
ADAM_STEP = 10
PER_EXAMPLE_BATCH_AXIS = {'x': 0, 'loss_target': 0}
SHARED_INPUTS = []
_WEIGHT_DTYPES = {'w_in': _jnp.float32, 'b_gate': _jnp.float32, 'rel_bias': _jnp.float32, 'w_attn_out': _jnp.float32, 'conv_dw_w': _jnp.float32, 'conv_dw_b': _jnp.float32, 'conv_ln_g': _jnp.float32, 'conv_ln_b': _jnp.float32, 'conv_pw_w': _jnp.float32, 'w_out': _jnp.float32, 'norm_mix_pre': _jnp.float32, 'norm_mix_post': _jnp.float32, 'norm_ffn_pre': _jnp.float32, 'norm_ffn_post': _jnp.float32, 'w_up': _jnp.float32, 'ffn_conv_w': _jnp.float32, 'ffn_conv_b': _jnp.float32, 'w_down': _jnp.float32}
MOMENT_SCALE = {'w_in': 1.325966e-01, 'b_gate': 2.364772e-01, 'rel_bias': 1.404272e-01, 'w_attn_out': 1.000021e-01, 'conv_dw_w': 3.686606e-01, 'conv_dw_b': 3.250995e+00, 'conv_ln_g': 1.215226e+00, 'conv_ln_b': 1.825555e+00, 'conv_pw_w': 7.466284e-01, 'w_out': 7.401641e-01, 'norm_mix_pre': 3.835881e-01, 'norm_mix_post': 1.626959e+01, 'norm_ffn_pre': 5.134337e-01, 'norm_ffn_post': 1.603380e+01, 'w_up': 2.391286e-01, 'ffn_conv_w': 2.839532e-01, 'ffn_conv_b': 1.061702e+00, 'w_down': 5.267296e-01}


def _to_microbatches(a, axis):
    t = _jnp.moveaxis(a, axis, 0)
    t = t.reshape((N_MICROBATCH, t.shape[0] // N_MICROBATCH) + t.shape[1:])
    return _jnp.moveaxis(t, 1, axis + 1)


def setup_inputs(seed: int = 0) -> dict:
    inp = _fwd_setup_inputs(seed)
    key = _jax.random.fold_in(_jax.random.key(seed), 7919)
    shape, _ = _output_shape()
    out = dict(inp)
    out["loss_target"] = _jax.random.normal(_jax.random.fold_in(key, 0), shape, _jnp.float32)
    for i, name in enumerate(TWIN_WEIGHTS):
        w = inp[name].astype(_jnp.float32)
        if MOMENT_SCALE is None:
            s = _jnp.sqrt(_jnp.mean(_jnp.square(w)) + 1e-30)
        else:
            s = MOMENT_SCALE[name]
        km, kv = _jax.random.split(_jax.random.fold_in(key, i + 1))
        out[name] = w
        out["m_" + name] = s * _jax.random.normal(km, w.shape, _jnp.float32)
        out["v_" + name] = (s * s) * _jax.random.uniform(kv, w.shape, _jnp.float32, 0.5, 1.5)
    if N_MICROBATCH > 1:
        for name, axis in PER_EXAMPLE_BATCH_AXIS.items():
            out[name] = _to_microbatches(out[name], axis)
    return {'x': out['x'], 'w_in': out['w_in'], 'b_gate': out['b_gate'], 'rel_bias': out['rel_bias'], 'w_attn_out': out['w_attn_out'], 'conv_dw_w': out['conv_dw_w'], 'conv_dw_b': out['conv_dw_b'], 'conv_ln_g': out['conv_ln_g'], 'conv_ln_b': out['conv_ln_b'], 'conv_pw_w': out['conv_pw_w'], 'w_out': out['w_out'], 'norm_mix_pre': out['norm_mix_pre'], 'norm_mix_post': out['norm_mix_post'], 'norm_ffn_pre': out['norm_ffn_pre'], 'norm_ffn_post': out['norm_ffn_post'], 'w_up': out['w_up'], 'ffn_conv_w': out['ffn_conv_w'], 'ffn_conv_b': out['ffn_conv_b'], 'w_down': out['w_down'], 'loss_target': out['loss_target'], 'm_w_in': out['m_w_in'], 'm_b_gate': out['m_b_gate'], 'm_rel_bias': out['m_rel_bias'], 'm_w_attn_out': out['m_w_attn_out'], 'm_conv_dw_w': out['m_conv_dw_w'], 'm_conv_dw_b': out['m_conv_dw_b'], 'm_conv_ln_g': out['m_conv_ln_g'], 'm_conv_ln_b': out['m_conv_ln_b'], 'm_conv_pw_w': out['m_conv_pw_w'], 'm_w_out': out['m_w_out'], 'm_norm_mix_pre': out['m_norm_mix_pre'], 'm_norm_mix_post': out['m_norm_mix_post'], 'm_norm_ffn_pre': out['m_norm_ffn_pre'], 'm_norm_ffn_post': out['m_norm_ffn_post'], 'm_w_up': out['m_w_up'], 'm_ffn_conv_w': out['m_ffn_conv_w'], 'm_ffn_conv_b': out['m_ffn_conv_b'], 'm_w_down': out['m_w_down'], 'v_w_in': out['v_w_in'], 'v_b_gate': out['v_b_gate'], 'v_rel_bias': out['v_rel_bias'], 'v_w_attn_out': out['v_w_attn_out'], 'v_conv_dw_w': out['v_conv_dw_w'], 'v_conv_dw_b': out['v_conv_dw_b'], 'v_conv_ln_g': out['v_conv_ln_g'], 'v_conv_ln_b': out['v_conv_ln_b'], 'v_conv_pw_w': out['v_conv_pw_w'], 'v_w_out': out['v_w_out'], 'v_norm_mix_pre': out['v_norm_mix_pre'], 'v_norm_mix_post': out['v_norm_mix_post'], 'v_norm_ffn_pre': out['v_norm_ffn_pre'], 'v_norm_ffn_post': out['v_norm_ffn_post'], 'v_w_up': out['v_w_up'], 'v_ffn_conv_w': out['v_ffn_conv_w'], 'v_ffn_conv_b': out['v_ffn_conv_b'], 'v_w_down': out['v_w_down']}


def _loss(weights, diff, rest, loss_target):
    with _jax.named_scope("forward"):
        args = {**rest, TWIN_DIFF_INPUT: diff, **{k: w.astype(_WEIGHT_DTYPES[k]) for k, w in weights.items()}}
        y = _forward(args)
    with _jax.named_scope("loss_head"):
        err = _jnp.square(y.astype(_jnp.float32) - loss_target)
        return 0.5 * _jnp.sum(_jnp.mean(err, axis=-1)) if err.ndim else 0.5 * err


def _adamw(w, g, m, v):
    m = ADAM_B1 * m + (1.0 - ADAM_B1) * g
    v = ADAM_B2 * v + (1.0 - ADAM_B2) * _jnp.square(g)
    m_hat = m / (1.0 - ADAM_B1 ** ADAM_STEP)
    v_hat = v / (1.0 - ADAM_B2 ** ADAM_STEP)
    delta = -ADAM_LR * (m_hat / (_jnp.sqrt(v_hat) + ADAM_EPS) + ADAM_WD * w)
    return delta, m, v


def reference(x, w_in, b_gate, rel_bias, w_attn_out, conv_dw_w, conv_dw_b, conv_ln_g, conv_ln_b, conv_pw_w, w_out, norm_mix_pre, norm_mix_post, norm_ffn_pre, norm_ffn_post, w_up, ffn_conv_w, ffn_conv_b, w_down, loss_target, m_w_in, m_b_gate, m_rel_bias, m_w_attn_out, m_conv_dw_w, m_conv_dw_b, m_conv_ln_g, m_conv_ln_b, m_conv_pw_w, m_w_out, m_norm_mix_pre, m_norm_mix_post, m_norm_ffn_pre, m_norm_ffn_post, m_w_up, m_ffn_conv_w, m_ffn_conv_b, m_w_down, v_w_in, v_b_gate, v_rel_bias, v_w_attn_out, v_conv_dw_w, v_conv_dw_b, v_conv_ln_g, v_conv_ln_b, v_conv_pw_w, v_w_out, v_norm_mix_pre, v_norm_mix_post, v_norm_ffn_pre, v_norm_ffn_post, v_w_up, v_ffn_conv_w, v_ffn_conv_b, v_w_down):
    given = dict(x=x, w_in=w_in, b_gate=b_gate, rel_bias=rel_bias, w_attn_out=w_attn_out, conv_dw_w=conv_dw_w, conv_dw_b=conv_dw_b, conv_ln_g=conv_ln_g, conv_ln_b=conv_ln_b, conv_pw_w=conv_pw_w, w_out=w_out, norm_mix_pre=norm_mix_pre, norm_mix_post=norm_mix_post, norm_ffn_pre=norm_ffn_pre, norm_ffn_post=norm_ffn_post, w_up=w_up, ffn_conv_w=ffn_conv_w, ffn_conv_b=ffn_conv_b, w_down=w_down, loss_target=loss_target, m_w_in=m_w_in, m_b_gate=m_b_gate, m_rel_bias=m_rel_bias, m_w_attn_out=m_w_attn_out, m_conv_dw_w=m_conv_dw_w, m_conv_dw_b=m_conv_dw_b, m_conv_ln_g=m_conv_ln_g, m_conv_ln_b=m_conv_ln_b, m_conv_pw_w=m_conv_pw_w, m_w_out=m_w_out, m_norm_mix_pre=m_norm_mix_pre, m_norm_mix_post=m_norm_mix_post, m_norm_ffn_pre=m_norm_ffn_pre, m_norm_ffn_post=m_norm_ffn_post, m_w_up=m_w_up, m_ffn_conv_w=m_ffn_conv_w, m_ffn_conv_b=m_ffn_conv_b, m_w_down=m_w_down, v_w_in=v_w_in, v_b_gate=v_b_gate, v_rel_bias=v_rel_bias, v_w_attn_out=v_w_attn_out, v_conv_dw_w=v_conv_dw_w, v_conv_dw_b=v_conv_dw_b, v_conv_ln_g=v_conv_ln_g, v_conv_ln_b=v_conv_ln_b, v_conv_pw_w=v_conv_pw_w, v_w_out=v_w_out, v_norm_mix_pre=v_norm_mix_pre, v_norm_mix_post=v_norm_mix_post, v_norm_ffn_pre=v_norm_ffn_pre, v_norm_ffn_post=v_norm_ffn_post, v_w_up=v_w_up, v_ffn_conv_w=v_ffn_conv_w, v_ffn_conv_b=v_ffn_conv_b, v_w_down=v_w_down)
    weights = {n: given[n] for n in TWIN_WEIGHTS}
    shared = {n: given[n] for n in SHARED_INPUTS}
    per_example = {n: given[n] for n in ['x']}
    grad_fn = _jax.value_and_grad(_loss, argnums=(0, 1))

    def one_microbatch(ex, loss_target):
        ex = dict(ex)
        diff = ex.pop(TWIN_DIFF_INPUT)
        return grad_fn(weights, diff, {**shared, **ex}, loss_target)

    if N_MICROBATCH == 1:
        loss, (grad_w, grad_x) = one_microbatch(per_example, given["loss_target"])
    else:
        def body(carry, xs):
            loss_sum, grad_sum = carry
            l_k, (gw_k, gx_k) = one_microbatch(xs[0], xs[1])
            with _jax.named_scope("update"):
                return (loss_sum + l_k, _jax.tree.map(_jnp.add, grad_sum, gw_k)), gx_k

        init = (_jnp.zeros((), _jnp.float32), _jax.tree.map(_jnp.zeros_like, weights))
        (loss, grad_w), grad_x = _jax.lax.scan(body, init, (per_example, given["loss_target"]))
    with _jax.named_scope("update"):
        delta_w, new_m, new_v = {}, {}, {}
        for n in TWIN_WEIGHTS:
            delta_w[n], new_m[n], new_v[n] = _adamw(weights[n], grad_w[n], given["m_" + n], given["v_" + n])
    return (loss, grad_x, *[grad_w[n] for n in TWIN_WEIGHTS], *[delta_w[n] for n in TWIN_WEIGHTS],
            *[new_m[n] for n in TWIN_WEIGHTS], *[new_v[n] for n in TWIN_WEIGHTS])
```

```python
import functools
import math

import jax
import jax.numpy as jnp
import numpy as np
from jax import lax
from jax.experimental import pallas as pl
from jax.experimental.pallas import tpu as pltpu

F32 = jnp.float32
BF16 = jnp.bfloat16
MESH = pl.DeviceIdType.MESH

HEAD_DIM = 128
HEADS_PER_GROUP = 4
DILATED_PATTERNS = ((128, 1), (512, 4), (2048, 16))
N_GROUPS = 3
N_HEADS = N_GROUPS * HEADS_PER_GROUP
SPAN = 128
GROUP_WIDTH = HEADS_PER_GROUP * HEAD_DIM
CONV_WIDTH = 31
FFN_CONV_WIDTH = 3
N_BUCKETS = 32
MAX_DISTANCE = 2048
RMS_EPS = 1e-6
LN_EPS = 1e-5
NEG_INF = -1e30
ADAM_LR = 0.001
ADAM_B1 = 0.9
ADAM_B2 = 0.999
ADAM_EPS = 1e-08
ADAM_WD = 0.01
ADAM_STEP = 10

LANES = 128
SUBLANES = 8
ROW_TILE = 256
TIME_BLOCK = 128
CONV_PAD = 32
FFN_PAD = 8
VMEM_LIMIT = 56 << 20


def _params(sem=None, vmem=None):
    kw = {}
    if sem is not None:
        kw["dimension_semantics"] = sem
    if vmem is not None:
        kw["vmem_limit_bytes"] = vmem
    return pltpu.CompilerParams(**kw)


def _pick(n, cands):
    for c in cands:
        if n % c == 0:
            return c
    return n


ELEMENTWISE_TILE_BYTES = 1 << 20


def _row_tile(rows, cols):
    for c in (512, 256, 128, 64, 32, 16):
        if rows % c == 0 and c * cols * 4 <= ELEMENTWISE_TILE_BYTES:
            return c
    return 16 if rows % 16 == 0 else 8


def _matmul(a, b, mode, out_dtype, name, tm=1024, tn=512, tk=1024):
    if mode == "nn":
        (M, K), (K2, N) = a.shape, b.shape
    elif mode == "nt":
        (M, K), (N, K2) = a.shape, b.shape
    else:
        (K, M), (K2, N) = a.shape, b.shape
    assert K == K2, (a.shape, b.shape, mode)
    tm = _pick(M, (tm, 512, 256, 128))
    tn = _pick(N, (tn, 512, 256, 128))
    tk = _pick(K, (tk, 512, 256, 128))
    nk = K // tk
    dn = {"nn": (((1,), (0,)), ((), ())), "nt": (((1,), (1,)), ((), ())), "tn": (((0,), (0,)), ((), ()))}[mode]

    def body(a_ref, b_ref, o_ref, acc_ref):
        k = pl.program_id(2)
        part = lax.dot_general(a_ref[...].astype(BF16), b_ref[...].astype(BF16), dn, preferred_element_type=F32)

        @pl.when(k == 0)
        def _():
            acc_ref[...] = part

        @pl.when(k > 0)
        def _():
            acc_ref[...] += part

        @pl.when(k == nk - 1)
        def _():
            o_ref[...] = acc_ref[...].astype(out_dtype)

    if mode == "tn":
        a_spec = pl.BlockSpec((tk, tm), lambda i, j, k: (k, i))
    else:
        a_spec = pl.BlockSpec((tm, tk), lambda i, j, k: (i, k))
    if mode == "nt":
        b_spec = pl.BlockSpec((tn, tk), lambda i, j, k: (j, k))
    else:
        b_spec = pl.BlockSpec((tk, tn), lambda i, j, k: (k, j))
    return pl.pallas_call(
        body, name=name, grid=(M // tm, N // tn, nk),
        in_specs=[a_spec, b_spec],
        out_specs=pl.BlockSpec((tm, tn), lambda i, j, k: (i, j)),
        out_shape=jax.ShapeDtypeStruct((M, N), out_dtype),
        scratch_shapes=[pltpu.VMEM((tm, tn), F32)],
        compiler_params=_params(("parallel", "parallel", "arbitrary"), VMEM_LIMIT),
    )(a, b)


def _rms(x, g):
    r = lax.rsqrt(jnp.mean(x * x, axis=-1, keepdims=True) + RMS_EPS)
    return x * r * g


def _rms_bwd(x, g, dy):
    r = lax.rsqrt(jnp.mean(x * x, axis=-1, keepdims=True) + RMS_EPS)
    n = x * r
    dn = dy * g
    dx = r * (dn - n * jnp.mean(dn * n, axis=-1, keepdims=True))
    return dx, jnp.sum(dy * n, axis=0, keepdims=True)


def _sigmoid(x):
    return 1.0 / (1.0 + jnp.exp(-x))


_GELU_C = math.sqrt(2.0 / math.pi)


def _gelu(x):
    return 0.5 * x * (1.0 + jnp.tanh(_GELU_C * (x + 0.044715 * x * x * x)))


def _gelu_grad(x):
    t = jnp.tanh(_GELU_C * (x + 0.044715 * x * x * x))
    return 0.5 * (1.0 + t) + 0.5 * x * (1.0 - t * t) * _GELU_C * (1.0 + 3.0 * 0.044715 * x * x)


def _row_spec(width, col_block=0):
    return pl.BlockSpec((ROW_TILE, width), lambda i: (i, col_block))


def _vec_spec(width, col_block=0):
    return pl.BlockSpec((1, width), lambda i: (0, col_block))


def _accumulate(ref, part):
    @pl.when(pl.program_id(0) == 0)
    def _():
        ref[...] = part

    @pl.when(pl.program_id(0) > 0)
    def _():
        ref[...] += part


def _rms_fwd_call(x, g):
    S, D = x.shape

    def body(x_ref, g_ref, h_ref):
        h_ref[...] = _rms(x_ref[...], g_ref[...])

    return pl.pallas_call(
        body, name="rms_mix_pre", grid=(S // ROW_TILE,),
        in_specs=[_row_spec(D), _vec_spec(D)], out_specs=_row_spec(D),
        out_shape=jax.ShapeDtypeStruct((S, D), F32),
        compiler_params=_params(("parallel",)),
    )(x, g)


def _ln_silu_call(c1, g, b):
    S, C = c1.shape

    def body(c_ref, g_ref, b_ref, o_ref):
        xv = c_ref[...]
        mu = jnp.mean(xv, axis=-1, keepdims=True)
        xc = xv - mu
        var = jnp.mean(xc * xc, axis=-1, keepdims=True)
        z = xc * lax.rsqrt(var + LN_EPS) * g_ref[...] + b_ref[...]
        o_ref[...] = z * _sigmoid(z)

    return pl.pallas_call(
        body, name="conv_ln_silu", grid=(S // ROW_TILE,),
        in_specs=[_row_spec(C), _vec_spec(C), _vec_spec(C)], out_specs=_row_spec(C),
        out_shape=jax.ShapeDtypeStruct((S, C), F32),
        compiler_params=_params(("parallel",)),
    )(c1, g, b)


def _ln_silu_bwd_call(c1, g, b, dc):
    S, C = c1.shape

    def body(c_ref, g_ref, b_ref, dc_ref, dx_ref, dg_ref, db_ref):
        xv = c_ref[...]
        mu = jnp.mean(xv, axis=-1, keepdims=True)
        xc = xv - mu
        rs = lax.rsqrt(jnp.mean(xc * xc, axis=-1, keepdims=True) + LN_EPS)
        xh = xc * rs
        z = xh * g_ref[...] + b_ref[...]
        sg = _sigmoid(z)
        dz = dc_ref[...] * (sg * (1.0 + z * (1.0 - sg)))
        dxh = dz * g_ref[...]
        dx_ref[...] = rs * (dxh - jnp.mean(dxh, axis=-1, keepdims=True) - xh * jnp.mean(dxh * xh, axis=-1, keepdims=True))
        _accumulate(dg_ref, jnp.sum(dz * xh, axis=0, keepdims=True))
        _accumulate(db_ref, jnp.sum(dz, axis=0, keepdims=True))

    return pl.pallas_call(
        body, name="conv_ln_silu_bwd", grid=(S // ROW_TILE,),
        in_specs=[_row_spec(C), _vec_spec(C), _vec_spec(C), _row_spec(C)],
        out_specs=[_row_spec(C), _vec_spec(C), _vec_spec(C)],
        out_shape=[jax.ShapeDtypeStruct((S, C), F32), jax.ShapeDtypeStruct((1, C), F32), jax.ShapeDtypeStruct((1, C), F32)],
        compiler_params=_params(("arbitrary",)),
    )(c1, g, b, dc)


def _mix_call(proj, gate_col0, b_gate, y_a, y_c):
    S, D = y_a.shape
    w = 512
    nc = D // w
    ga0, gc0 = gate_col0 // w, (gate_col0 + D) // w

    def body(ga_ref, gc_ref, ba_ref, bc_ref, ya_ref, yc_ref, o_ref):
        o_ref[...] = (_sigmoid(ga_ref[...] + ba_ref[...]) * ya_ref[...]
                      + _sigmoid(gc_ref[...] + bc_ref[...]) * yc_ref[...])

    tile = lambda off: pl.BlockSpec((ROW_TILE, w), lambda i, j: (i, off + j))
    vec = lambda off: pl.BlockSpec((1, w), lambda i, j: (0, off + j))
    return pl.pallas_call(
        body, name="gate_mix", grid=(S // ROW_TILE, nc),
        in_specs=[tile(ga0), tile(gc0), vec(0), vec(nc), tile(0), tile(0)],
        out_specs=tile(0), out_shape=jax.ShapeDtypeStruct((S, D), F32),
        compiler_params=_params(("parallel", "parallel")),
    )(proj, proj, b_gate, b_gate, y_a, y_c)


def _mix_bwd_call(dmixed, proj, gate_col0, b_gate, y_a, y_c):
    S, D = y_a.shape
    w = 512
    nc = D // w
    ga0, gc0 = gate_col0 // w, (gate_col0 + D) // w

    def body(dm_ref, ga_ref, gc_ref, ba_ref, bc_ref, ya_ref, yc_ref, dya_ref, dyc_ref, dga_ref, dgc_ref, dba_ref, dbc_ref):
        dm = dm_ref[...]
        sa = _sigmoid(ga_ref[...] + ba_ref[...])
        sc = _sigmoid(gc_ref[...] + bc_ref[...])
        dya_ref[...] = dm * sa
        dyc_ref[...] = dm * sc
        dga = dm * ya_ref[...] * sa * (1.0 - sa)
        dgc = dm * yc_ref[...] * sc * (1.0 - sc)
        dga_ref[...] = dga.astype(BF16)
        dgc_ref[...] = dgc.astype(BF16)
        pa = jnp.sum(dga, axis=0, keepdims=True)
        pc = jnp.sum(dgc, axis=0, keepdims=True)

        @pl.when(pl.program_id(1) == 0)
        def _():
            dba_ref[...] = pa
            dbc_ref[...] = pc

        @pl.when(pl.program_id(1) > 0)
        def _():
            dba_ref[...] += pa
            dbc_ref[...] += pc

    tile = lambda off: pl.BlockSpec((ROW_TILE, w), lambda j, i: (i, off + j))
    vec = lambda off: pl.BlockSpec((1, w), lambda j, i: (0, off + j))
    return pl.pallas_call(
        body, name="gate_mix_bwd", grid=(nc, S // ROW_TILE),
        in_specs=[tile(0), tile(ga0), tile(gc0), vec(0), vec(nc), tile(0), tile(0)],
        out_specs=[tile(0), tile(0), tile(0), tile(0), vec(0), vec(0)],
        out_shape=[jax.ShapeDtypeStruct((S, D), F32), jax.ShapeDtypeStruct((S, D), F32),
                   jax.ShapeDtypeStruct((S, D), BF16), jax.ShapeDtypeStruct((S, D), BF16),
                   jax.ShapeDtypeStruct((1, D), F32), jax.ShapeDtypeStruct((1, D), F32)],
        compiler_params=_params(("parallel", "arbitrary")),
    )(dmixed, proj, proj, b_gate, b_gate, y_a, y_c)


def _res1_call(x, out, g_post, g_pre):
    S, D = x.shape

    def body(x_ref, o_ref, gp_ref, gq_ref, x1_ref, h2_ref):
        x1 = x_ref[...] + _rms(o_ref[...], gp_ref[...])
        x1_ref[...] = x1
        h2_ref[...] = _rms(x1, gq_ref[...])

    return pl.pallas_call(
        body, name="residual_mix", grid=(S // ROW_TILE,),
        in_specs=[_row_spec(D), _row_spec(D), _vec_spec(D), _vec_spec(D)],
        out_specs=[_row_spec(D), _row_spec(D)],
        out_shape=[jax.ShapeDtypeStruct((S, D), F32)] * 2,
        compiler_params=_params(("parallel",)),
    )(x, out, g_post, g_pre)


def _loss_call(y, x1, g_post, target):
    S, D = y.shape

    def body(y_ref, x1_ref, g_ref, t_ref, loss_ref, dx_ref):
        err = x1_ref[...] + _rms(y_ref[...], g_ref[...]) - t_ref[...]
        dx_ref[...] = err * (1.0 / D)
        part = 0.5 * jnp.sum(jnp.mean(err * err, axis=-1, keepdims=True), axis=0, keepdims=True)
        _accumulate(loss_ref, jnp.broadcast_to(part, (SUBLANES, LANES)))

    return pl.pallas_call(
        body, name="residual_ffn_loss", grid=(S // ROW_TILE,),
        in_specs=[_row_spec(D), _row_spec(D), _vec_spec(D), _row_spec(D)],
        out_specs=[pl.BlockSpec((SUBLANES, LANES), lambda i: (0, 0)), _row_spec(D)],
        out_shape=[jax.ShapeDtypeStruct((SUBLANES, LANES), F32), jax.ShapeDtypeStruct((S, D), F32)],
        compiler_params=_params(("arbitrary",)),
    )(y, x1, g_post, target)


def _rms_bwd_call(x, g, dy, name):
    S, D = x.shape

    def body(x_ref, g_ref, dy_ref, dx_ref, dg_ref):
        dx, dg = _rms_bwd(x_ref[...], g_ref[...], dy_ref[...])
        dx_ref[...] = dx
        _accumulate(dg_ref, dg)

    return pl.pallas_call(
        body, name=name, grid=(S // ROW_TILE,),
        in_specs=[_row_spec(D), _vec_spec(D), _row_spec(D)],
        out_specs=[_row_spec(D), _vec_spec(D)],
        out_shape=[jax.ShapeDtypeStruct((S, D), F32), jax.ShapeDtypeStruct((1, D), F32)],
        compiler_params=_params(("arbitrary",)),
    )(x, g, dy)


def _mid_bwd_call(x1, g_pre, dh2, dx2, out, g_post):
    S, D = x1.shape

    def body(x1_ref, gq_ref, dh_ref, dx2_ref, o_ref, gp_ref, dx1_ref, do_ref, dgq_ref, dgp_ref):
        d, dgq = _rms_bwd(x1_ref[...], gq_ref[...], dh_ref[...])
        dx1 = dx2_ref[...] + d
        dx1_ref[...] = dx1
        do, dgp = _rms_bwd(o_ref[...], gp_ref[...], dx1)
        do_ref[...] = do
        _accumulate(dgq_ref, dgq)
        _accumulate(dgp_ref, dgp)

    return pl.pallas_call(
        body, name="residual_mix_bwd", grid=(S // ROW_TILE,),
        in_specs=[_row_spec(D), _vec_spec(D), _row_spec(D), _row_spec(D), _row_spec(D), _vec_spec(D)],
        out_specs=[_row_spec(D), _row_spec(D), _vec_spec(D), _vec_spec(D)],
        out_shape=[jax.ShapeDtypeStruct((S, D), F32)] * 2 + [jax.ShapeDtypeStruct((1, D), F32)] * 2,
        compiler_params=_params(("arbitrary",)),
    )(x1, g_pre, dh2, dx2, out, g_post)


def _in_bwd_call(x, g, dh1, dx1):
    S, D = x.shape

    def body(x_ref, g_ref, dh_ref, dx1_ref, gx_ref, dg_ref):
        d, dg = _rms_bwd(x_ref[...], g_ref[...], dh_ref[...])
        gx_ref[...] = dx1_ref[...] + d
        _accumulate(dg_ref, dg)

    return pl.pallas_call(
        body, name="rms_mix_pre_bwd", grid=(S // ROW_TILE,),
        in_specs=[_row_spec(D), _vec_spec(D), _row_spec(D), _row_spec(D)],
        out_specs=[_row_spec(D), _vec_spec(D)],
        out_shape=[jax.ShapeDtypeStruct((S, D), F32), jax.ShapeDtypeStruct((1, D), F32)],
        compiler_params=_params(("arbitrary",)),
    )(x, g, dh1, dx1)


def _bucket_table(dilation):
    qi = np.arange(SPAN)[:, None]
    ki = np.arange(2 * SPAN)[None, :]
    dist = np.maximum(qi + SPAN - ki, 0) * dilation
    max_exact = N_BUCKETS // 2
    d = np.maximum(dist, 1).astype(np.float64)
    large = max_exact + (np.log(d / max_exact) / math.log(MAX_DISTANCE / max_exact) * (N_BUCKETS - max_exact)).astype(np.int32)
    large = np.minimum(large, N_BUCKETS - 1)
    return np.where(dist < max_exact, dist, large).astype(np.int32)


def _bucket_tables():
    return jnp.asarray(np.stack([_bucket_table(r) for _, r in DILATED_PATTERNS]))


def _bias_table_call(rel_bias, buckets):
    def body(rb_ref, bk_ref, o_ref):
        for h in range(N_HEADS):
            bk = bk_ref[h // HEADS_PER_GROUP]

            def step(b, acc):
                return jnp.where(bk == b, rb_ref[b, h], acc)

            o_ref[h] = lax.fori_loop(0, N_BUCKETS, step, jnp.zeros((SPAN, 2 * SPAN), F32))

    return pl.pallas_call(
        body, name="rel_bias_table",
        in_specs=[pl.BlockSpec(memory_space=pltpu.SMEM), pl.BlockSpec(memory_space=pltpu.VMEM)],
        out_specs=pl.BlockSpec(memory_space=pltpu.VMEM),
        out_shape=jax.ShapeDtypeStruct((N_HEADS, SPAN, 2 * SPAN), F32),
    )(rel_bias, buckets)


def _bias_grad_call(dbias, buckets):
    def body(db_ref, bk_ref, o_ref):
        for h in range(N_HEADS):
            bk = bk_ref[h // HEADS_PER_GROUP]
            dv = db_ref[h]

            def step(b, carry):
                o_ref[b, h] = jnp.sum(jnp.where(bk == b, dv, 0.0))
                return carry

            lax.fori_loop(0, N_BUCKETS, step, 0)

    return pl.pallas_call(
        body, name="rel_bias_grad",
        in_specs=[pl.BlockSpec(memory_space=pltpu.VMEM), pl.BlockSpec(memory_space=pltpu.VMEM)],
        out_specs=pl.BlockSpec(memory_space=pltpu.SMEM),
        out_shape=jax.ShapeDtypeStruct((N_BUCKETS, N_HEADS), F32),
    )(dbias, buckets)


def _dot_nt(a, b):
    return lax.dot_general(a, b, (((1,), (1,)), ((), ())), preferred_element_type=F32)


def _dot_nn(a, b):
    return lax.dot_general(a, b, (((1,), (0,)), ((), ())), preferred_element_type=F32)


def _dot_tn(a, b):
    return lax.dot_general(a, b, (((0,), (0,)), ((), ())), preferred_element_type=F32)


def _band_masks(n, nb):
    qi = lax.broadcasted_iota(jnp.int32, (SPAN, SPAN), 0)
    ki = lax.broadcasted_iota(jnp.int32, (SPAN, SPAN), 1)
    prev_ok = jnp.logical_and(ki >= qi, n > 0)
    cur_ok = ki <= qi
    next_ok = jnp.logical_and(ki >= qi, n < nb - 1)
    return prev_ok, cur_ok, next_ok


def _regroup_call(src, col_block, r, inverse, name):
    S = src.shape[0]
    L = S // r
    nt = GROUP_WIDTH // LANES

    def body(x_ref, o_ref):
        for rho in range(r):
            if inverse:
                o_ref[pl.ds(rho, L, stride=r), :] = x_ref[rho * L:(rho + 1) * L, :]
            else:
                o_ref[rho * L:(rho + 1) * L, :] = x_ref[pl.ds(rho, L, stride=r), :]

    return pl.pallas_call(
        body, name=name, grid=(nt,),
        in_specs=[pl.BlockSpec((S, LANES), lambda i: (0, col_block * nt + i))],
        out_specs=pl.BlockSpec((S, LANES), lambda i: (0, i)),
        out_shape=jax.ShapeDtypeStruct((S, GROUP_WIDTH), F32),
        compiler_params=_params(("parallel",)),
    )(src)


def _to_group_order(src, col_block, group, name):
    r = DILATED_PATTERNS[group][1]
    if r == 1:
        return src, col_block
    return _regroup_call(src, col_block, r, False, name), 0


def _to_token_order(arr, group, name):
    r = DILATED_PATTERNS[group][1]
    return arr if r == 1 else _regroup_call(arr, 0, r, True, name)


def _attn_fwd_call(q, k, v, bias, group):
    S = q[0].shape[0]
    r = DILATED_PATTERNS[group][1]
    nb = S // r // SPAN
    scale = HEAD_DIM ** -0.5

    def body(q_ref, kp_ref, kc_ref, vp_ref, vc_ref, b_ref, o_ref, m_ref, l_ref):
        n = pl.program_id(1)
        prev_ok, cur_ok, _ = _band_masks(n, nb)
        for j in range(HEADS_PER_GROUP):
            sl = slice(j * HEAD_DIM, (j + 1) * HEAD_DIM)
            q = q_ref[:, sl].astype(BF16)
            sp = _dot_nt(q, kp_ref[:, sl].astype(BF16)) * scale + b_ref[j, :, :SPAN]
            sc = _dot_nt(q, kc_ref[:, sl].astype(BF16)) * scale + b_ref[j, :, SPAN:]
            sp = jnp.where(prev_ok, sp, NEG_INF)
            sc = jnp.where(cur_ok, sc, NEG_INF)
            m = jnp.maximum(jnp.max(sp, axis=-1, keepdims=True), jnp.max(sc, axis=-1, keepdims=True))
            pp = jnp.exp(sp - m)
            pc = jnp.exp(sc - m)
            den = jnp.sum(pp, axis=-1, keepdims=True) + jnp.sum(pc, axis=-1, keepdims=True)
            o_ref[:, sl] = (_dot_nn(pp.astype(BF16), vp_ref[:, sl].astype(BF16))
                            + _dot_nn(pc.astype(BF16), vc_ref[:, sl].astype(BF16)))
            m_ref[:, sl] = jnp.broadcast_to(m, (SPAN, HEAD_DIM))
            l_ref[:, sl] = jnp.broadcast_to(den, (SPAN, HEAD_DIM))

    blk = (SPAN, GROUP_WIDTH)
    cur = lambda cb: pl.BlockSpec(blk, lambda rho, n: (rho * nb + n, cb))
    prev = lambda cb: pl.BlockSpec(blk, lambda rho, n: (rho * nb + jnp.maximum(n - 1, 0), cb))
    return pl.pallas_call(
        body, name=f"attn_fwd_g{group}", grid=(r, nb),
        in_specs=[cur(q[1]), prev(k[1]), cur(k[1]), prev(v[1]), cur(v[1]),
                  pl.BlockSpec((HEADS_PER_GROUP, SPAN, 2 * SPAN), lambda rho, n: (group, 0, 0))],
        out_specs=[cur(0)] * 3,
        out_shape=[jax.ShapeDtypeStruct((S, GROUP_WIDTH), F32)] * 3,
        compiler_params=_params(("parallel", "parallel")),
    )(q[0], k[0], k[0], v[0], v[0], bias)


def _attn_merge_call(parts):
    S = parts[0].shape[0]

    def body(o1, m1, l1, o2, m2, l2, o3, m3, l3, a_ref, lse_ref):
        mx = jnp.maximum(jnp.maximum(m1[...], m2[...]), m3[...])
        w1 = jnp.exp(m1[...] - mx)
        w2 = jnp.exp(m2[...] - mx)
        w3 = jnp.exp(m3[...] - mx)
        den = w1 * l1[...] + w2 * l2[...] + w3 * l3[...]
        a_ref[...] = (w1 * o1[...] + w2 * o2[...] + w3 * o3[...]) / den
        lse_ref[...] = mx + jnp.log(den)

    return pl.pallas_call(
        body, name="attn_merge", grid=(S // ROW_TILE,),
        in_specs=[_row_spec(GROUP_WIDTH)] * 9, out_specs=[_row_spec(GROUP_WIDTH)] * 2,
        out_shape=[jax.ShapeDtypeStruct((S, GROUP_WIDTH), F32)] * 2,
        compiler_params=_params(("parallel",)),
    )(*parts)


def _attn_delta_call(a, da):
    S = a.shape[0]

    def body(a_ref, da_ref, d_ref):
        for j in range(HEADS_PER_GROUP):
            sl = slice(j * HEAD_DIM, (j + 1) * HEAD_DIM)
            d = jnp.sum(a_ref[:, sl] * da_ref[:, sl], axis=-1, keepdims=True)
            d_ref[:, sl] = jnp.broadcast_to(d, (ROW_TILE, HEAD_DIM))

    return pl.pallas_call(
        body, name="attn_delta", grid=(S // ROW_TILE,),
        in_specs=[_row_spec(GROUP_WIDTH)] * 2, out_specs=_row_spec(GROUP_WIDTH),
        out_shape=jax.ShapeDtypeStruct((S, GROUP_WIDTH), F32),
        compiler_params=_params(("parallel",)),
    )(a, da)


def _attn_bwd_call(q, k, v, bias, da, lse, delta, group):
    S = q[0].shape[0]
    r = DILATED_PATTERNS[group][1]
    nb = S // r // SPAN
    scale = HEAD_DIM ** -0.5

    def body(q_ref, qn_ref, kp_ref, kc_ref, vp_ref, vc_ref, b_ref, da_ref, dan_ref, lse_ref, lsen_ref, dl_ref, dln_ref,
             dq_ref, dk_ref, dv_ref, db_ref):
        n = pl.program_id(1)
        prev_ok, cur_ok, next_ok = _band_masks(n, nb)
        first = jnp.logical_and(pl.program_id(0) == 0, n == 0)
        for j in range(HEADS_PER_GROUP):
            sl = slice(j * HEAD_DIM, (j + 1) * HEAD_DIM)
            q = q_ref[:, sl].astype(BF16)
            qn = qn_ref[:, sl].astype(BF16)
            kp = kp_ref[:, sl].astype(BF16)
            kc = kc_ref[:, sl].astype(BF16)
            vp = vp_ref[:, sl].astype(BF16)
            vc = vc_ref[:, sl].astype(BF16)
            dav = da_ref[:, sl].astype(BF16)
            dan = dan_ref[:, sl].astype(BF16)
            bp = b_ref[j, :, :SPAN]
            bc = b_ref[j, :, SPAN:]
            pp = jnp.exp(jnp.where(prev_ok, _dot_nt(q, kp) * scale + bp, NEG_INF) - lse_ref[:, sl])
            pc = jnp.exp(jnp.where(cur_ok, _dot_nt(q, kc) * scale + bc, NEG_INF) - lse_ref[:, sl])
            pn = jnp.exp(jnp.where(next_ok, _dot_nt(qn, kc) * scale + bp, NEG_INF) - lsen_ref[:, sl])
            dsp = pp * (_dot_nt(dav, vp) - dl_ref[:, sl])
            dsc = pc * (_dot_nt(dav, vc) - dl_ref[:, sl])
            dsn = pn * (_dot_nt(dan, vc) - dln_ref[:, sl])
            dsp_b, dsc_b, dsn_b = dsp.astype(BF16), dsc.astype(BF16), dsn.astype(BF16)
            dq_ref[:, sl] = (_dot_nn(dsp_b, kp) + _dot_nn(dsc_b, kc)) * scale
            dk_ref[:, sl] = (_dot_tn(dsc_b, q) + _dot_tn(dsn_b, qn)) * scale
            dv_ref[:, sl] = _dot_tn(pc.astype(BF16), dav) + _dot_tn(pn.astype(BF16), dan)

            @pl.when(first)
            def _():
                db_ref[j, :, :SPAN] = dsp
                db_ref[j, :, SPAN:] = dsc

            @pl.when(jnp.logical_not(first))
            def _():
                db_ref[j, :, :SPAN] += dsp
                db_ref[j, :, SPAN:] += dsc

    blk = (SPAN, GROUP_WIDTH)
    cur = lambda cb: pl.BlockSpec(blk, lambda rho, n: (rho * nb + n, cb))
    prev = lambda cb: pl.BlockSpec(blk, lambda rho, n: (rho * nb + jnp.maximum(n - 1, 0), cb))
    nxt = lambda cb: pl.BlockSpec(blk, lambda rho, n: (rho * nb + jnp.minimum(n + 1, nb - 1), cb))
    band = (HEADS_PER_GROUP, SPAN, 2 * SPAN)
    return pl.pallas_call(
        body, name=f"attn_bwd_g{group}", grid=(r, nb),
        in_specs=[cur(q[1]), nxt(q[1]), prev(k[1]), cur(k[1]), prev(v[1]), cur(v[1]),
                  pl.BlockSpec(band, lambda rho, n: (group, 0, 0)),
                  cur(0), nxt(0), cur(0), nxt(0), cur(0), nxt(0)],
        out_specs=[cur(0), cur(0), cur(0), pl.BlockSpec(band, lambda rho, n: (0, 0, 0))],
        out_shape=[jax.ShapeDtypeStruct((S, GROUP_WIDTH), F32)] * 3 + [jax.ShapeDtypeStruct(band, F32)],
        compiler_params=_params(("arbitrary", "arbitrary")),
    )(q[0], q[0], k[0], k[0], v[0], v[0], bias, da, da, lse, lse, delta, delta)


def _taps(width):
    return [(k, (width - 1 - k) // SUBLANES, (width - 1 - k) % SUBLANES) for k in range(width)]


def _shifted(win, width, pad, up):
    total = win.shape[0]
    for b in range(SUBLANES):
        taps = [(k, a) for k, a, bb in _taps(width) if bb == b]
        if not taps:
            continue
        if up:
            rolled = win if b == 0 else pltpu.roll(win, total - b, axis=0)
        else:
            rolled = win if b == 0 else pltpu.roll(win, b, axis=0)
        for k, a in taps:
            start = SUBLANES * a if up else pad - SUBLANES * a
            yield k, rolled[start:start + TIME_BLOCK, :]


def _conv_block(win, w_ref, width, pad):
    acc = None
    for k, rows in _shifted(win, width, pad, up=False):
        term = w_ref[k:k + 1, :] * rows
        acc = term if acc is None else acc + term
    return acc


def _conv_transpose_block(win, w_ref, width, pad):
    acc = None
    for k, rows in _shifted(win, width, pad, up=True):
        term = w_ref[k:k + 1, :] * rows
        acc = term if acc is None else acc + term
    return acc


def _conv_weight_grad(win, dy, dw_ref, width, pad):
    for k, rows in _shifted(win, width, pad, up=False):
        dw_ref[k:k + 1, :] += jnp.sum(dy * rows, axis=0, keepdims=True)


def _time_loop(S, step):
    def it(tb, carry):
        step(pl.multiple_of(tb * TIME_BLOCK, TIME_BLOCK))
        return carry

    lax.fori_loop(0, S // TIME_BLOCK, it, 0)


def _conv_fwd_call(proj, col0, w, b):
    S = proj.shape[0]
    C = w.shape[1]
    nt = C // LANES
    v0, g0 = col0 // LANES, (col0 + C) // LANES

    def body(val_ref, gate_ref, w_ref, b_ref, o_ref, pad_ref):
        pad_ref[0:CONV_PAD, :] = jnp.zeros((CONV_PAD, LANES), F32)
        pad_ref[CONV_PAD:, :] = val_ref[...] * _sigmoid(gate_ref[...])

        def step(t0):
            win = pad_ref[pl.ds(t0, TIME_BLOCK + CONV_PAD), :]
            o_ref[pl.ds(t0, TIME_BLOCK), :] = _conv_block(win, w_ref, CONV_WIDTH, CONV_PAD) + b_ref[...]

        _time_loop(S, step)

    seq = lambda off: pl.BlockSpec((S, LANES), lambda i: (0, off + i))
    return pl.pallas_call(
        body, name="conv_module", grid=(nt,),
        in_specs=[seq(v0), seq(g0), pl.BlockSpec((CONV_WIDTH, LANES), lambda i: (0, i)), pl.BlockSpec((1, LANES), lambda i: (0, i))],
        out_specs=seq(0), out_shape=jax.ShapeDtypeStruct((S, C), F32),
        scratch_shapes=[pltpu.VMEM((S + CONV_PAD, LANES), F32)],
        compiler_params=_params(("parallel",)),
    )(proj, proj, w, b)


def _conv_bwd_call(proj, col0, w, dc1):
    S = proj.shape[0]
    C = w.shape[1]
    nt = C // LANES
    v0, g0 = col0 // LANES, (col0 + C) // LANES

    def body(val_ref, gate_ref, w_ref, dy_ref, dval_ref, dgate_ref, dw_ref, db_ref, xpad_ref, dpad_ref, dwacc_ref):
        xpad_ref[0:CONV_PAD, :] = jnp.zeros((CONV_PAD, LANES), F32)
        xpad_ref[CONV_PAD:, :] = val_ref[...] * _sigmoid(gate_ref[...])
        dpad_ref[0:S, :] = dy_ref[...]
        dpad_ref[S:, :] = jnp.zeros((CONV_PAD, LANES), F32)
        dwacc_ref[...] = jnp.zeros_like(dwacc_ref)

        def step(t0):
            rows = pl.ds(t0, TIME_BLOCK)
            _conv_weight_grad(xpad_ref[pl.ds(t0, TIME_BLOCK + CONV_PAD), :], dy_ref[rows, :], dwacc_ref, CONV_WIDTH, CONV_PAD)
            dc0 = _conv_transpose_block(dpad_ref[pl.ds(t0, TIME_BLOCK + CONV_PAD), :], w_ref, CONV_WIDTH, CONV_PAD)
            sg = _sigmoid(gate_ref[rows, :])
            dval_ref[rows, :] = (dc0 * sg).astype(BF16)
            dgate_ref[rows, :] = (dc0 * val_ref[rows, :] * sg * (1.0 - sg)).astype(BF16)

        _time_loop(S, step)
        dw_ref[...] = dwacc_ref[...]
        db_ref[...] = jnp.sum(dy_ref[...], axis=0, keepdims=True)

    seq = lambda off: pl.BlockSpec((S, LANES), lambda i: (0, off + i))
    return pl.pallas_call(
        body, name="conv_module_bwd", grid=(nt,),
        in_specs=[seq(v0), seq(g0), pl.BlockSpec((CONV_WIDTH, LANES), lambda i: (0, i)), seq(0)],
        out_specs=[seq(0), seq(0), pl.BlockSpec((CONV_PAD, LANES), lambda i: (0, i)), pl.BlockSpec((1, LANES), lambda i: (0, i))],
        out_shape=[jax.ShapeDtypeStruct((S, C), BF16), jax.ShapeDtypeStruct((S, C), BF16),
                   jax.ShapeDtypeStruct((CONV_PAD, C), F32), jax.ShapeDtypeStruct((1, C), F32)],
        scratch_shapes=[pltpu.VMEM((S + CONV_PAD, LANES), F32), pltpu.VMEM((S + CONV_PAD, LANES), F32),
                        pltpu.VMEM((CONV_PAD, LANES), F32)],
        compiler_params=_params(("parallel",)),
    )(proj, proj, w, dc1)


def _ffn_fwd_call(u, w, b):
    S, C2 = u.shape
    C = C2 // 2
    nt = C // LANES

    def body(ug_ref, uv_ref, wg_ref, wv_ref, bg_ref, bv_ref, f_ref, pg_ref, pv_ref):
        zeros = jnp.zeros((FFN_PAD, LANES), F32)
        pg_ref[0:FFN_PAD, :] = zeros
        pv_ref[0:FFN_PAD, :] = zeros
        pg_ref[FFN_PAD:, :] = ug_ref[...]
        pv_ref[FFN_PAD:, :] = uv_ref[...]

        def step(t0):
            win = pl.ds(t0, TIME_BLOCK + FFN_PAD)
            cg = _conv_block(pg_ref[win, :], wg_ref, FFN_CONV_WIDTH, FFN_PAD) + bg_ref[...]
            cv = _conv_block(pv_ref[win, :], wv_ref, FFN_CONV_WIDTH, FFN_PAD) + bv_ref[...]
            f_ref[pl.ds(t0, TIME_BLOCK), :] = _gelu(cg) * cv

        _time_loop(S, step)

    seq = lambda off: pl.BlockSpec((S, LANES), lambda i: (0, off + i))
    wsp = lambda off: pl.BlockSpec((FFN_CONV_WIDTH, LANES), lambda i: (0, off + i))
    bsp = lambda off: pl.BlockSpec((1, LANES), lambda i: (0, off + i))
    return pl.pallas_call(
        body, name="ffn_conv_geglu", grid=(nt,),
        in_specs=[seq(0), seq(nt), wsp(0), wsp(nt), bsp(0), bsp(nt)],
        out_specs=seq(0), out_shape=jax.ShapeDtypeStruct((S, C), F32),
        scratch_shapes=[pltpu.VMEM((S + FFN_PAD, LANES), F32)] * 2,
        compiler_params=_params(("parallel",)),
    )(u, u, w, w, b, b)


def _ffn_bwd_call(u, w, b, df):
    S, C2 = u.shape
    C = C2 // 2
    nt = C // LANES

    def body(ug_ref, uv_ref, wg_ref, wv_ref, bg_ref, bv_ref, df_ref,
             dug_ref, duv_ref, dwg_ref, dwv_ref, dbg_ref, dbv_ref,
             pg_ref, pv_ref, dg_ref, dv_ref, dwg_acc, dwv_acc, dbg_acc, dbv_acc):
        zeros = jnp.zeros((FFN_PAD, LANES), F32)
        pg_ref[0:FFN_PAD, :] = zeros
        pv_ref[0:FFN_PAD, :] = zeros
        pg_ref[FFN_PAD:, :] = ug_ref[...]
        pv_ref[FFN_PAD:, :] = uv_ref[...]
        dg_ref[S:, :] = zeros
        dv_ref[S:, :] = zeros
        dwg_acc[...] = jnp.zeros_like(dwg_acc)
        dwv_acc[...] = jnp.zeros_like(dwv_acc)
        dbg_acc[...] = jnp.zeros_like(dbg_acc)
        dbv_acc[...] = jnp.zeros_like(dbv_acc)

        def first(t0):
            win = pl.ds(t0, TIME_BLOCK + FFN_PAD)
            rows = pl.ds(t0, TIME_BLOCK)
            xg = pg_ref[win, :]
            xv = pv_ref[win, :]
            cg = _conv_block(xg, wg_ref, FFN_CONV_WIDTH, FFN_PAD) + bg_ref[...]
            cv = _conv_block(xv, wv_ref, FFN_CONV_WIDTH, FFN_PAD) + bv_ref[...]
            dfb = df_ref[rows, :]
            dcg = dfb * cv * _gelu_grad(cg)
            dcv = dfb * _gelu(cg)
            dg_ref[rows, :] = dcg
            dv_ref[rows, :] = dcv
            _conv_weight_grad(xg, dcg, dwg_acc, FFN_CONV_WIDTH, FFN_PAD)
            _conv_weight_grad(xv, dcv, dwv_acc, FFN_CONV_WIDTH, FFN_PAD)
            dbg_acc[...] += jnp.sum(dcg, axis=0, keepdims=True)
            dbv_acc[...] += jnp.sum(dcv, axis=0, keepdims=True)

        def second(t0):
            win = pl.ds(t0, TIME_BLOCK + FFN_PAD)
            rows = pl.ds(t0, TIME_BLOCK)
            dug_ref[rows, :] = _conv_transpose_block(dg_ref[win, :], wg_ref, FFN_CONV_WIDTH, FFN_PAD).astype(BF16)
            duv_ref[rows, :] = _conv_transpose_block(dv_ref[win, :], wv_ref, FFN_CONV_WIDTH, FFN_PAD).astype(BF16)

        _time_loop(S, first)
        _time_loop(S, second)
        dwg_ref[...] = dwg_acc[...]
        dwv_ref[...] = dwv_acc[...]
        dbg_ref[...] = dbg_acc[...]
        dbv_ref[...] = dbv_acc[...]

    seq = lambda off: pl.BlockSpec((S, LANES), lambda i: (0, off + i))
    wsp = lambda off: pl.BlockSpec((FFN_CONV_WIDTH, LANES), lambda i: (0, off + i))
    bsp = lambda off: pl.BlockSpec((1, LANES), lambda i: (0, off + i))
    return pl.pallas_call(
        body, name="ffn_conv_geglu_bwd", grid=(nt,),
        in_specs=[seq(0), seq(nt), wsp(0), wsp(nt), bsp(0), bsp(nt), seq(0)],
        out_specs=[seq(0), seq(0), pl.BlockSpec((SUBLANES, LANES), lambda i: (0, i)), pl.BlockSpec((SUBLANES, LANES), lambda i: (0, i)),
                   bsp(0), bsp(0)],
        out_shape=[jax.ShapeDtypeStruct((S, C), BF16)] * 2 + [jax.ShapeDtypeStruct((SUBLANES, C), F32)] * 2
        + [jax.ShapeDtypeStruct((1, C), F32)] * 2,
        scratch_shapes=[pltpu.VMEM((S + FFN_PAD, LANES), F32)] * 4 + [pltpu.VMEM((SUBLANES, LANES), F32)] * 2
        + [pltpu.VMEM((1, LANES), F32)] * 2,
        compiler_params=_params(("parallel",)),
    )(u, u, w, w, b, b, df)


def _adamw_call(w, g, m, v, name):
    R, C = w.shape
    tr = _row_tile(R, C)
    c1 = 1.0 / (1.0 - ADAM_B1 ** ADAM_STEP)
    c2 = 1.0 / (1.0 - ADAM_B2 ** ADAM_STEP)

    def body(w_ref, g_ref, m_ref, v_ref, d_ref, mo_ref, vo_ref):
        gv = g_ref[...]
        mn = ADAM_B1 * m_ref[...] + (1.0 - ADAM_B1) * gv
        vn = ADAM_B2 * v_ref[...] + (1.0 - ADAM_B2) * (gv * gv)
        mo_ref[...] = mn
        vo_ref[...] = vn
        d_ref[...] = -ADAM_LR * ((mn * c1) / (jnp.sqrt(vn * c2) + ADAM_EPS) + ADAM_WD * w_ref[...])

    spec = pl.BlockSpec((tr, C), lambda i: (i, 0))
    return pl.pallas_call(
        body, name=name, grid=(R // tr,),
        in_specs=[spec] * 4, out_specs=[spec] * 3,
        out_shape=[jax.ShapeDtypeStruct((R, C), F32)] * 3,
        compiler_params=_params(("parallel",)),
    )(w, g, m, v)


def _position():
    return lax.axis_index("x"), lax.axis_index("y"), lax.axis_index("c")


def _chip_peers(x, y):
    return [(x, 1 - y), (1 - x, y), (1 - x, 1 - y)]


def _part(ref, axis, index, size):
    align = SUBLANES if axis == 0 else LANES
    start = pl.multiple_of(index * size, align)
    return ref.at[pl.ds(start, size), :] if axis == 0 else ref.at[:, pl.ds(start, size)]


ANY = pl.BlockSpec(memory_space=pl.ANY)


def _allgather_call(shards, axes):
    n = len(shards)
    fulls = [jax.ShapeDtypeStruct(tuple(s.shape[d] * (4 if d == ax else 1) for d in range(2)), s.dtype)
             for s, ax in zip(shards, axes)]

    def body(*refs):
        ins, outs = refs[:n], refs[n:2 * n]
        send_sems, recv_sems, local_sems = refs[2 * n:]
        x, y, c = _position()
        chip = 2 * x + y
        copies = []
        for i in range(n):
            place = _part(outs[i], axes[i], chip, shards[i].shape[axes[i]])
            local = pltpu.make_async_copy(ins[i], place, local_sems.at[i])
            local.start()
            copies.append(local)
            for k, (px, py) in enumerate(_chip_peers(x, y)):
                cp = pltpu.make_async_remote_copy(src_ref=ins[i], dst_ref=place, send_sem=send_sems.at[i, k],
                                                  recv_sem=recv_sems.at[i, k], device_id=(px, py, c), device_id_type=MESH)
                cp.start()
                copies.append(cp)
        for cp in copies:
            cp.wait()

    return pl.pallas_call(
        body, name="weight_allgather",
        in_specs=[ANY] * n, out_specs=[ANY] * n, out_shape=fulls,
        scratch_shapes=[pltpu.SemaphoreType.DMA((n, 3)), pltpu.SemaphoreType.DMA((n, 3)), pltpu.SemaphoreType.DMA((n,))],
    )(*shards)


def _half_shape(shape, axis):
    return (shape[0] // 2, shape[1]) if axis == 1 else (shape[0], shape[1] // 2)


def _sibling_exchange_call(grads, axes):
    n = len(grads)
    halves = [jax.ShapeDtypeStruct(_half_shape(g.shape, ax), F32) for g, ax in zip(grads, axes)]

    def body(*refs):
        ins, outs = refs[:n], refs[n:2 * n]
        send_sems, recv_sems = refs[2 * n:]
        x, y, c = _position()
        copies = []
        for i in range(n):
            hs = _half_shape(grads[i].shape, axes[i])
            src = _part(ins[i], 1 - axes[i], 1 - c, hs[1 - axes[i]])
            cp = pltpu.make_async_remote_copy(src_ref=src, dst_ref=outs[i], send_sem=send_sems.at[i], recv_sem=recv_sems.at[i],
                                              device_id=(x, y, 1 - c), device_id_type=MESH)
            cp.start()
            copies.append(cp)
        for cp in copies:
            cp.wait()

    return pl.pallas_call(
        body, name="grad_sibling_exchange",
        in_specs=[ANY] * n, out_specs=[ANY] * n, out_shape=halves,
        scratch_shapes=[pltpu.SemaphoreType.DMA((n,)), pltpu.SemaphoreType.DMA((n,))],
    )(*grads)


def _pair_sum_call(grad, recv, axis, core, name):
    hr, hc = recv.shape
    tr = _row_tile(hr, hc)

    def body(core_ref, g_ref, r_ref, o_ref, ob_ref):
        s = g_ref[...] + r_ref[...]
        o_ref[...] = s
        ob_ref[...] = s.astype(BF16)

    if axis == 1:
        g_spec = pl.BlockSpec((tr, hc), lambda i, core_ref: (core_ref[0] * (hr // tr) + i, 0))
    else:
        g_spec = pl.BlockSpec((tr, hc), lambda i, core_ref: (i, core_ref[0]))
    spec = pl.BlockSpec((tr, hc), lambda i, core_ref: (i, 0))
    return pl.pallas_call(
        body, name=name,
        grid_spec=pltpu.PrefetchScalarGridSpec(num_scalar_prefetch=1, grid=(hr // tr,), in_specs=[g_spec, spec], out_specs=[spec, spec]),
        out_shape=[jax.ShapeDtypeStruct((hr, hc), F32), jax.ShapeDtypeStruct((hr, hc), BF16)],
        compiler_params=_params(("parallel",)),
    )(core, grad, recv)


def _chip_exchange_call(partials, axes):
    n = len(partials)
    pieces = []
    for t, ax in zip(partials, axes):
        pieces.append((t.shape[0], t.shape[1] // 4) if ax == 1 else (t.shape[0] // 4, t.shape[1]))
    outs_shape = [jax.ShapeDtypeStruct((3,) + p, BF16) for p in pieces]

    def body(*refs):
        ins, outs = refs[:n], refs[n:2 * n]
        send_sems, recv_sems = refs[2 * n:]
        x, y, c = _position()
        copies = []
        for i in range(n):
            for k, (px, py) in enumerate(_chip_peers(x, y)):
                src = _part(ins[i], axes[i], 2 * px + py, pieces[i][axes[i]])
                cp = pltpu.make_async_remote_copy(src_ref=src, dst_ref=outs[i].at[k], send_sem=send_sems.at[i, k],
                                                  recv_sem=recv_sems.at[i, k], device_id=(px, py, c), device_id_type=MESH)
                cp.start()
                copies.append(cp)
        for cp in copies:
            cp.wait()

    return pl.pallas_call(
        body, name="grad_chip_exchange",
        in_specs=[ANY] * n, out_specs=[ANY] * n, out_shape=outs_shape,
        scratch_shapes=[pltpu.SemaphoreType.DMA((n, 3)), pltpu.SemaphoreType.DMA((n, 3))],
    )(*partials)


def _chip_sum_call(partial, recv, axis, chip, name):
    _, pr, pc = recv.shape
    tr = _row_tile(pr, pc)

    def body(chip_ref, p_ref, r_ref, o_ref):
        o_ref[...] = ((p_ref[...] + r_ref[0].astype(F32)) + r_ref[1].astype(F32)) + r_ref[2].astype(F32)

    if axis == 1:
        p_spec = pl.BlockSpec((tr, pc), lambda i, chip_ref: (i, chip_ref[0]))
    else:
        p_spec = pl.BlockSpec((tr, pc), lambda i, chip_ref: (chip_ref[0] * (pr // tr) + i, 0))
    return pl.pallas_call(
        body, name=name,
        grid_spec=pltpu.PrefetchScalarGridSpec(
            num_scalar_prefetch=1, grid=(pr // tr,),
            in_specs=[p_spec, pl.BlockSpec((3, tr, pc), lambda i, chip_ref: (0, i, 0))],
            out_specs=pl.BlockSpec((tr, pc), lambda i, chip_ref: (i, 0))),
        out_shape=jax.ShapeDtypeStruct((pr, pc), F32),
        compiler_params=_params(("parallel",)),
    )(chip, partial, recv)


def _sibling_assemble_call(halves, axes):
    n = len(halves)
    shards = [jax.ShapeDtypeStruct((h.shape[0] * 2, h.shape[1]) if ax == 1 else (h.shape[0], h.shape[1] * 2), F32)
              for h, ax in zip(halves, axes)]

    def body(*refs):
        ins, outs = refs[:n], refs[n:2 * n]
        send_sems, recv_sems, local_sems = refs[2 * n:]
        x, y, c = _position()
        copies = []
        for i in range(n):
            place = _part(outs[i], 1 - axes[i], c, halves[i].shape[1 - axes[i]])
            local = pltpu.make_async_copy(ins[i], place, local_sems.at[i])
            local.start()
            copies.append(local)
            cp = pltpu.make_async_remote_copy(src_ref=ins[i], dst_ref=place, send_sem=send_sems.at[i], recv_sem=recv_sems.at[i],
                                              device_id=(x, y, 1 - c), device_id_type=MESH)
            cp.start()
            copies.append(cp)
        for cp in copies:
            cp.wait()

    return pl.pallas_call(
        body, name="grad_sibling_assemble",
        in_specs=[ANY] * n, out_specs=[ANY] * n, out_shape=shards,
        scratch_shapes=[pltpu.SemaphoreType.DMA((n,)), pltpu.SemaphoreType.DMA((n,)), pltpu.SemaphoreType.DMA((n,))],
    )(*halves)


def _small_allreduce_call(packed):
    rows = packed.shape[0]

    def body(x_ref, o_ref, buf_ref, send_sems, recv_sems):
        x, y, c = _position()
        me = 4 * x + 2 * y + c
        buf_ref[me] = x_ref[...]
        copies = []
        for k in range(1, 8):
            peer = (1 - x if k & 4 else x, 1 - y if k & 2 else y, 1 - c if k & 1 else c)
            cp = pltpu.make_async_remote_copy(src_ref=buf_ref.at[me], dst_ref=buf_ref.at[me], send_sem=send_sems.at[k - 1],
                                              recv_sem=recv_sems.at[k - 1], device_id=peer, device_id_type=MESH)
            cp.start()
            copies.append(cp)
        for cp in copies:
            cp.wait()
        acc = buf_ref[0]
        for d in range(1, 8):
            acc = acc + buf_ref[d]
        o_ref[...] = acc

    return pl.pallas_call(
        body, name="small_grad_allreduce",
        in_specs=[pl.BlockSpec(memory_space=pltpu.VMEM)], out_specs=pl.BlockSpec(memory_space=pltpu.VMEM),
        out_shape=jax.ShapeDtypeStruct((rows, LANES), F32),
        scratch_shapes=[pltpu.VMEM((8, rows, LANES), F32), pltpu.SemaphoreType.DMA((7,)), pltpu.SemaphoreType.DMA((7,))],
    )(packed)


def _pack(arrays):
    flat = jnp.concatenate([a.reshape(-1).astype(F32) for a in arrays])
    rows = -(-flat.shape[0] // LANES)
    rows = -(-rows // SUBLANES) * SUBLANES
    flat = jnp.pad(flat, (0, rows * LANES - flat.shape[0]))
    return flat.reshape(rows, LANES)


def _unpack(packed, shapes):
    flat = packed.reshape(-1)
    out, off = [], 0
    for shp in shapes:
        size = int(np.prod(shp))
        out.append(flat[off:off + size].reshape(shp))
        off += size
    return out


def _local_step(xs, target, P):
    S, D = xs.shape
    qkv_width = 3 * N_HEADS * HEAD_DIM
    glu_col0, gate_col0 = qkv_width, qkv_width + 2 * D

    h1 = _rms_fwd_call(xs, P["norm_mix_pre"])
    proj = _matmul(h1, P["w_in"], "nn", F32, "proj_in")
    buckets = _bucket_tables()
    bias = _bias_table_call(P["rel_bias"], buckets)
    qkv, parts = [], []
    for g in range(N_GROUPS):
        q, k, v = [_to_group_order(proj, 3 * t + g, g, f"group_order_{'qkv'[t]}{g}") for t in range(3)]
        qkv.append((q, k, v))
        o_g, m_g, l_g = _attn_fwd_call(q, k, v, bias, g)
        parts += [_to_token_order(t, g, f"token_order_{n}{g}") for t, n in ((o_g, "o"), (m_g, "m"), (l_g, "l"))]
    a, lse = _attn_merge_call(parts)
    y_a = _matmul(a, P["w_attn_out"], "nn", F32, "attn_out")
    c1 = _conv_fwd_call(proj, glu_col0, P["conv_dw_w"], P["conv_dw_b"])
    cact = _ln_silu_call(c1, P["conv_ln_g"], P["conv_ln_b"])
    y_c = _matmul(cact, P["conv_pw_w"], "nn", F32, "conv_pw")
    mixed = _mix_call(proj, gate_col0, P["b_gate"], y_a, y_c)
    out = _matmul(mixed, P["w_out"], "nn", F32, "mix_out")
    x1, h2 = _res1_call(xs, out, P["norm_mix_post"], P["norm_ffn_pre"])
    u = _matmul(h2, P["w_up"], "nn", F32, "ffn_up")
    f = _ffn_fwd_call(u, P["ffn_conv_w"], P["ffn_conv_b"])
    yff = _matmul(f, P["w_down"], "nn", F32, "ffn_down")
    loss_tile, dx2 = _loss_call(yff, x1, P["norm_ffn_post"], target)

    G = {}
    dyff, G["norm_ffn_post"] = _rms_bwd_call(yff, P["norm_ffn_post"], dx2, "rms_ffn_post_bwd")
    df = _matmul(dyff, P["w_down"], "nt", F32, "ffn_down_dx")
    G["w_down"] = _matmul(f, dyff, "tn", F32, "ffn_down_dw")
    dug, duv, dwg, dwv, dbg, dbv = _ffn_bwd_call(u, P["ffn_conv_w"], P["ffn_conv_b"], df)
    G["ffn_conv_w"] = jnp.concatenate([dwg[:FFN_CONV_WIDTH], dwv[:FFN_CONV_WIDTH]], axis=1)
    G["ffn_conv_b"] = jnp.concatenate([dbg, dbv], axis=1)
    du = jnp.concatenate([dug, duv], axis=1)
    dh2 = _matmul(du, P["w_up"], "nt", F32, "ffn_up_dx")
    G["w_up"] = _matmul(h2, du, "tn", F32, "ffn_up_dw")
    dx1, dout, G["norm_ffn_pre"], G["norm_mix_post"] = _mid_bwd_call(x1, P["norm_ffn_pre"], dh2, dx2, out, P["norm_mix_post"])
    dmixed = _matmul(dout, P["w_out"], "nt", F32, "mix_out_dx")
    G["w_out"] = _matmul(mixed, dout, "tn", F32, "mix_out_dw")
    dya, dyc, dga, dgc, dba, dbc = _mix_bwd_call(dmixed, proj, gate_col0, P["b_gate"], y_a, y_c)
    G["b_gate"] = jnp.concatenate([dba, dbc], axis=1)
    da = _matmul(dya, P["w_attn_out"], "nt", F32, "attn_out_dx")
    G["w_attn_out"] = _matmul(a, dya, "tn", F32, "attn_out_dw")
    dcact = _matmul(dyc, P["conv_pw_w"], "nt", F32, "conv_pw_dx")
    G["conv_pw_w"] = _matmul(cact, dyc, "tn", F32, "conv_pw_dw")
    dc1, G["conv_ln_g"], G["conv_ln_b"] = _ln_silu_bwd_call(c1, P["conv_ln_g"], P["conv_ln_b"], dcact)
    dval, dgate, dw_dw, G["conv_dw_b"] = _conv_bwd_call(proj, glu_col0, P["conv_dw_w"], dc1)
    G["conv_dw_w"] = dw_dw[:CONV_WIDTH]
    delta = _attn_delta_call(a, da)
    dqs, dks, dvs, dbs = [], [], [], []
    for g in range(N_GROUPS):
        grouped = [_to_group_order(t, 0, g, f"group_order_{n}{g}")[0] for t, n in ((da, "da"), (lse, "lse"), (delta, "delta"))]
        dq, dk, dv, db = _attn_bwd_call(*qkv[g], bias, *grouped, g)
        dqs.append(_to_token_order(dq, g, f"token_order_dq{g}"))
        dks.append(_to_token_order(dk, g, f"token_order_dk{g}"))
        dvs.append(_to_token_order(dv, g, f"token_order_dv{g}"))
        dbs.append(db)
    G["rel_bias"] = _bias_grad_call(jnp.concatenate(dbs, axis=0), buckets)
    dproj = jnp.concatenate([t.astype(BF16) for t in dqs + dks + dvs] + [dval, dgate, dga, dgc], axis=1)
    dh1 = _matmul(dproj, P["w_in"], "nt", F32, "proj_in_dx")
    G["w_in"] = _matmul(h1, dproj, "tn", F32, "proj_in_dw")
    grad_x, G["norm_mix_pre"] = _in_bwd_call(xs, P["norm_mix_pre"], dh1, dx1)
    return loss_tile, grad_x, G


def kernel(x, w_in, b_gate, rel_bias, w_attn_out, conv_dw_w, conv_dw_b, conv_ln_g, conv_ln_b, conv_pw_w, w_out, norm_mix_pre, norm_mix_post, norm_ffn_pre, norm_ffn_post, w_up, ffn_conv_w, ffn_conv_b, w_down, loss_target, m_w_in, m_b_gate, m_rel_bias, m_w_attn_out, m_conv_dw_w, m_conv_dw_b, m_conv_ln_g, m_conv_ln_b, m_conv_pw_w, m_w_out, m_norm_mix_pre, m_norm_mix_post, m_norm_ffn_pre, m_norm_ffn_post, m_w_up, m_ffn_conv_w, m_ffn_conv_b, m_w_down, v_w_in, v_b_gate, v_rel_bias, v_w_attn_out, v_conv_dw_w, v_conv_dw_b, v_conv_ln_g, v_conv_ln_b, v_conv_pw_w, v_w_out, v_norm_mix_pre, v_norm_mix_post, v_norm_ffn_pre, v_norm_ffn_post, v_w_up, v_ffn_conv_w, v_ffn_conv_b, v_w_down):
    weights = dict(w_in=w_in, b_gate=b_gate, rel_bias=rel_bias, w_attn_out=w_attn_out, conv_dw_w=conv_dw_w, conv_dw_b=conv_dw_b,
                   conv_ln_g=conv_ln_g, conv_ln_b=conv_ln_b, conv_pw_w=conv_pw_w, w_out=w_out, norm_mix_pre=norm_mix_pre,
                   norm_mix_post=norm_mix_post, norm_ffn_pre=norm_ffn_pre, norm_ffn_post=norm_ffn_post, w_up=w_up,
                   ffn_conv_w=ffn_conv_w, ffn_conv_b=ffn_conv_b, w_down=w_down)
    m_in = dict(w_in=m_w_in, b_gate=m_b_gate, rel_bias=m_rel_bias, w_attn_out=m_w_attn_out, conv_dw_w=m_conv_dw_w,
                conv_dw_b=m_conv_dw_b, conv_ln_g=m_conv_ln_g, conv_ln_b=m_conv_ln_b, conv_pw_w=m_conv_pw_w, w_out=m_w_out,
                norm_mix_pre=m_norm_mix_pre, norm_mix_post=m_norm_mix_post, norm_ffn_pre=m_norm_ffn_pre,
                norm_ffn_post=m_norm_ffn_post, w_up=m_w_up, ffn_conv_w=m_ffn_conv_w, ffn_conv_b=m_ffn_conv_b, w_down=m_w_down)
    v_in = dict(w_in=v_w_in, b_gate=v_b_gate, rel_bias=v_rel_bias, w_attn_out=v_w_attn_out, conv_dw_w=v_conv_dw_w,
                conv_dw_b=v_conv_dw_b, conv_ln_g=v_conv_ln_g, conv_ln_b=v_conv_ln_b, conv_pw_w=v_conv_pw_w, w_out=v_w_out,
                norm_mix_pre=v_norm_mix_pre, norm_mix_post=v_norm_mix_post, norm_ffn_pre=v_norm_ffn_pre,
                norm_ffn_post=v_norm_ffn_post, w_up=v_w_up, ffn_conv_w=v_ffn_conv_w, ffn_conv_b=v_ffn_conv_b, w_down=v_w_down)
    names = list(weights)
    xi, yi, ci = _position()
    chip = 2 * xi + yi
    chip_arr = jnp.reshape(chip, (1,)).astype(jnp.int32)
    core_arr = jnp.reshape(ci, (1,)).astype(jnp.int32)

    xs = x[0]
    target = loss_target[0]
    S, D = xs.shape

    big = ["w_in", "w_attn_out", "conv_pw_w", "w_out", "w_up", "w_down"]
    big_axis = dict(w_in=1, w_attn_out=1, conv_pw_w=0, w_out=0, w_up=1, w_down=0)
    shards = [weights[k][0].astype(BF16) for k in big] + [conv_dw_w[0], ffn_conv_w[0]]
    gathered = _allgather_call(shards, [big_axis[k] for k in big] + [1, 1])
    W = dict(zip(big, gathered[:6]))
    dw_full, fc_full = gathered[6], gathered[7]
    P = dict(W, conv_dw_w=dw_full, ffn_conv_w=fc_full, b_gate=b_gate, rel_bias=rel_bias, conv_dw_b=conv_dw_b,
             conv_ln_g=conv_ln_g, conv_ln_b=conv_ln_b, norm_mix_pre=norm_mix_pre, norm_mix_post=norm_mix_post,
             norm_ffn_pre=norm_ffn_pre, norm_ffn_post=norm_ffn_post, ffn_conv_b=ffn_conv_b)
    loss_tile, grad_x, G = _local_step(xs, target, P)
    loss = lax.psum(loss_tile[0, 0], ("x", "y", "c"))

    axes = [big_axis[k] for k in big]
    recv1 = _sibling_exchange_call([G[k] for k in big], axes)
    pair_f32, pair_bf16 = [], []
    for k, r1 in zip(big, recv1):
        s32, s16 = _pair_sum_call(G[k], r1, big_axis[k], core_arr, f"pair_sum_{k}")
        pair_f32.append(s32)
        pair_bf16.append(s16)
    recv2 = _chip_exchange_call(pair_bf16, axes)
    halves = [_chip_sum_call(p, r2, big_axis[k], chip_arr, f"chip_sum_{k}") for k, p, r2 in zip(big, pair_f32, recv2)]
    reduced = dict(zip(big, _sibling_assemble_call(halves, axes)))

    small = [k for k in names if k not in big]
    packed = _pack([G[k] for k in small])
    summed = _unpack(_small_allreduce_call(packed), [G[k].shape for k in small])
    for k, gsum in zip(small, summed):
        if k in ("conv_dw_w", "ffn_conv_w"):
            cols = weights[k].shape[2]
            reduced[k] = lax.dynamic_slice_in_dim(gsum, chip * cols, cols, axis=1)
        else:
            reduced[k] = gsum

    grads, deltas, new_m, new_v = {}, {}, {}, {}
    for k in big:
        d, mn, vn = _adamw_call(weights[k][0], reduced[k], m_in[k][0], v_in[k][0], f"adamw_{k}")
        grads[k], deltas[k], new_m[k], new_v[k] = reduced[k][None], d[None], mn[None], vn[None]
    flat2 = lambda t: t.reshape(-1, t.shape[-1]) if t.ndim == 3 else t
    pw = _pack([flat2(weights[k]) for k in small])
    pg = _pack([reduced[k] for k in small])
    pm = _pack([flat2(m_in[k]) for k in small])
    pv = _pack([flat2(v_in[k]) for k in small])
    pd, pmn, pvn = _adamw_call(pw, pg, pm, pv, "adamw_small")
    shapes = [weights[k].shape for k in small]
    for k, gk, dk_, mk, vk in zip(small, [reduced[k] for k in small], _unpack(pd, shapes), _unpack(pmn, shapes), _unpack(pvn, shapes)):
        grads[k], deltas[k], new_m[k], new_v[k] = gk.reshape(weights[k].shape), dk_, mk, vk

    return (loss, grad_x[None], *[grads[k] for k in names], *[deltas[k] for k in names],
            *[new_m[k] for k in names], *[new_v[k] for k in names])
```

```python
import functools
import math

import jax
import jax.numpy as jnp
import numpy as np
from jax import lax
from jax.experimental import pallas as pl
from jax.experimental.pallas import tpu as pltpu

F32 = jnp.float32
BF16 = jnp.bfloat16
MESH = pl.DeviceIdType.MESH

HEAD_DIM = 128
HEADS_PER_GROUP = 4
DILATED_PATTERNS = ((128, 1), (512, 4), (2048, 16))
N_GROUPS = 3
N_HEADS = N_GROUPS * HEADS_PER_GROUP
SPAN = 128
GROUP_WIDTH = HEADS_PER_GROUP * HEAD_DIM
CONV_WIDTH = 31
FFN_CONV_WIDTH = 3
N_BUCKETS = 32
MAX_DISTANCE = 2048
RMS_EPS = 1e-6
LN_EPS = 1e-5
NEG_INF = -1e30
ADAM_LR = 0.001
ADAM_B1 = 0.9
ADAM_B2 = 0.999
ADAM_EPS = 1e-08
ADAM_WD = 0.01
ADAM_STEP = 10

LANES = 128
SUBLANES = 8
ROW_TILE = 256
TIME_BLOCK = 128
CONV_PAD = 32
FFN_PAD = 8
VMEM_LIMIT = 56 << 20


def _params(sem=None, vmem=None):
    kw = {}
    if sem is not None:
        kw["dimension_semantics"] = sem
    if vmem is not None:
        kw["vmem_limit_bytes"] = vmem
    return pltpu.CompilerParams(**kw)


def _pick(n, cands):
    for c in cands:
        if n % c == 0:
            return c
    return n


ELEMENTWISE_TILE_BYTES = 1 << 20


def _row_tile(rows, cols):
    for c in (512, 256, 128, 64, 32, 16):
        if rows % c == 0 and c * cols * 4 <= ELEMENTWISE_TILE_BYTES:
            return c
    return 16 if rows % 16 == 0 else 8


N_CHIPS = 4
M_TILES = (1024, 1408, 512, 256, 128)
N_TILES = (512, 1408, 256, 128)
K_TILES = (2176, 2048, 1408, 1024, 512, 256, 128)


def _matmul(a, b, mode, name, out_shards=False, tm=None):
    assert a.dtype == BF16 and b.dtype == BF16, (name, a.dtype, b.dtype)
    b3 = b.ndim == 3
    tn = tk = None
    if mode == "nn":
        M, K = a.shape
        N = b.shape[-1] * (N_CHIPS if b3 else 1)
        tn = b.shape[-1] if b3 else None
    elif mode == "nt":
        M, K = a.shape
        N = b.shape[-2]
        tk = b.shape[-1] if b3 else None
    else:
        K, M = a.shape
        N = b.shape[1]
        tn = N // N_CHIPS if out_shards else None
    tm = tm or _pick(M, M_TILES)
    tn = tn or _pick(N, N_TILES)
    tk = tk or _pick(K, K_TILES)
    nk = K // tk
    dn = {"nn": (((1,), (0,)), ((), ())), "nt": (((1,), (1,)), ((), ())), "tn": (((0,), (0,)), ((), ()))}[mode]

    def body(a_ref, b_ref, o_ref):
        if nk == 1:
            o_ref[...] = lax.dot_general(a_ref[...], b_ref[...], dn, preferred_element_type=F32)
        else:
            @pl.when(pl.program_id(2) == 0)
            def _():
                o_ref[...] = jnp.zeros_like(o_ref)

            o_ref[...] += lax.dot_general(a_ref[...], b_ref[...], dn, preferred_element_type=F32)

    if mode == "tn":
        a_spec = pl.BlockSpec((tk, tm), lambda i, j, k: (k, i))
    else:
        a_spec = pl.BlockSpec((tm, tk), lambda i, j, k: (i, k))
    if mode == "nn":
        b_spec = pl.BlockSpec((None, tk, tn), lambda i, j, k: (j, k, 0)) if b3 else pl.BlockSpec((tk, tn), lambda i, j, k: (k, j))
    elif mode == "nt":
        b_spec = pl.BlockSpec((None, tn, tk), lambda i, j, k: (k, j, 0)) if b3 else pl.BlockSpec((tn, tk), lambda i, j, k: (j, k))
    else:
        b_spec = pl.BlockSpec((tk, tn), lambda i, j, k: (k, j))
    if out_shards:
        out_spec = pl.BlockSpec((None, tm, tn), lambda i, j, k: (j, i, 0))
        out_shape = jax.ShapeDtypeStruct((N_CHIPS, M, tn), F32)
    else:
        out_spec = pl.BlockSpec((tm, tn), lambda i, j, k: (i, j))
        out_shape = jax.ShapeDtypeStruct((M, N), F32)
    return pl.pallas_call(
        body, name=name, grid=(M // tm, N // tn, nk),
        in_specs=[a_spec, b_spec], out_specs=out_spec, out_shape=out_shape,
        compiler_params=_params(("parallel", "parallel", "arbitrary"), VMEM_LIMIT),
    )(a, b)


def _rms(x, g):
    r = lax.rsqrt(jnp.mean(x * x, axis=-1, keepdims=True) + RMS_EPS)
    return x * r * g


def _rms_bwd(x, g, dy):
    r = lax.rsqrt(jnp.mean(x * x, axis=-1, keepdims=True) + RMS_EPS)
    n = x * r
    dn = dy * g
    dx = r * (dn - n * jnp.mean(dn * n, axis=-1, keepdims=True))
    return dx, jnp.sum(dy * n, axis=0, keepdims=True)


def _sigmoid(x):
    return 1.0 / (1.0 + jnp.exp(-x))


_GELU_C = math.sqrt(2.0 / math.pi)


def _gelu(x):
    return 0.5 * x * (1.0 + jnp.tanh(_GELU_C * (x + 0.044715 * x * x * x)))


def _gelu_grad(x):
    t = jnp.tanh(_GELU_C * (x + 0.044715 * x * x * x))
    return 0.5 * (1.0 + t) + 0.5 * x * (1.0 - t * t) * _GELU_C * (1.0 + 3.0 * 0.044715 * x * x)


def _row_spec(width, col_block=0):
    return pl.BlockSpec((ROW_TILE, width), lambda i: (i, col_block))


def _vec_spec(width, col_block=0):
    return pl.BlockSpec((1, width), lambda i: (0, col_block))


def _accumulate(ref, part):
    @pl.when(pl.program_id(0) == 0)
    def _():
        ref[...] = part

    @pl.when(pl.program_id(0) > 0)
    def _():
        ref[...] += part


def _rms_fwd_call(x, g):
    S, D = x.shape

    def body(x_ref, g_ref, h_ref):
        h_ref[...] = _rms(x_ref[...], g_ref[...]).astype(BF16)

    return pl.pallas_call(
        body, name="rms_mix_pre", grid=(S // ROW_TILE,),
        in_specs=[_row_spec(D), _vec_spec(D)], out_specs=_row_spec(D),
        out_shape=jax.ShapeDtypeStruct((S, D), BF16),
        compiler_params=_params(("parallel",)),
    )(x, g)


def _ln_silu_call(c1, g, b):
    S, C = c1.shape

    def body(c_ref, g_ref, b_ref, o_ref):
        xv = c_ref[...]
        mu = jnp.mean(xv, axis=-1, keepdims=True)
        xc = xv - mu
        var = jnp.mean(xc * xc, axis=-1, keepdims=True)
        z = xc * lax.rsqrt(var + LN_EPS) * g_ref[...] + b_ref[...]
        o_ref[...] = (z * _sigmoid(z)).astype(BF16)

    return pl.pallas_call(
        body, name="conv_ln_silu", grid=(S // ROW_TILE,),
        in_specs=[_row_spec(C), _vec_spec(C), _vec_spec(C)], out_specs=_row_spec(C),
        out_shape=jax.ShapeDtypeStruct((S, C), BF16),
        compiler_params=_params(("parallel",)),
    )(c1, g, b)


def _ln_silu_bwd_call(c1, g, b, dc):
    S, C = c1.shape

    def body(c_ref, g_ref, b_ref, dc_ref, dx_ref, dg_ref, db_ref):
        xv = c_ref[...]
        mu = jnp.mean(xv, axis=-1, keepdims=True)
        xc = xv - mu
        rs = lax.rsqrt(jnp.mean(xc * xc, axis=-1, keepdims=True) + LN_EPS)
        xh = xc * rs
        z = xh * g_ref[...] + b_ref[...]
        sg = _sigmoid(z)
        dz = dc_ref[...] * (sg * (1.0 + z * (1.0 - sg)))
        dxh = dz * g_ref[...]
        dx_ref[...] = rs * (dxh - jnp.mean(dxh, axis=-1, keepdims=True) - xh * jnp.mean(dxh * xh, axis=-1, keepdims=True))
        _accumulate(dg_ref, jnp.sum(dz * xh, axis=0, keepdims=True))
        _accumulate(db_ref, jnp.sum(dz, axis=0, keepdims=True))

    return pl.pallas_call(
        body, name="conv_ln_silu_bwd", grid=(S // ROW_TILE,),
        in_specs=[_row_spec(C), _vec_spec(C), _vec_spec(C), _row_spec(C)],
        out_specs=[_row_spec(C), _vec_spec(C), _vec_spec(C)],
        out_shape=[jax.ShapeDtypeStruct((S, C), F32), jax.ShapeDtypeStruct((1, C), F32), jax.ShapeDtypeStruct((1, C), F32)],
        compiler_params=_params(("arbitrary",)),
    )(c1, g, b, dc)


def _mix_call(proj, gate_col0, b_gate, y_a, y_c):
    S, D = y_a.shape
    w = 512
    nc = D // w
    ga0, gc0 = gate_col0 // w, (gate_col0 + D) // w

    def body(ga_ref, gc_ref, ba_ref, bc_ref, ya_ref, yc_ref, o_ref):
        o_ref[...] = (_sigmoid(ga_ref[...] + ba_ref[...]) * ya_ref[...]
                      + _sigmoid(gc_ref[...] + bc_ref[...]) * yc_ref[...]).astype(BF16)

    tile = lambda off: pl.BlockSpec((ROW_TILE, w), lambda i, j: (i, off + j))
    vec = lambda off: pl.BlockSpec((1, w), lambda i, j: (0, off + j))
    return pl.pallas_call(
        body, name="gate_mix", grid=(S // ROW_TILE, nc),
        in_specs=[tile(ga0), tile(gc0), vec(0), vec(nc), tile(0), tile(0)],
        out_specs=tile(0), out_shape=jax.ShapeDtypeStruct((S, D), BF16),
        compiler_params=_params(("parallel", "parallel")),
    )(proj, proj, b_gate, b_gate, y_a, y_c)


def _mix_bwd_call(dmixed, proj, gate_col0, b_gate, y_a, y_c):
    S, D = y_a.shape
    w = 512
    nc = D // w
    ga0, gc0 = gate_col0 // w, (gate_col0 + D) // w

    def body(dm_ref, ga_ref, gc_ref, ba_ref, bc_ref, ya_ref, yc_ref, dya_ref, dyc_ref, dga_ref, dgc_ref, dba_ref, dbc_ref):
        dm = dm_ref[...]
        sa = _sigmoid(ga_ref[...] + ba_ref[...])
        sc = _sigmoid(gc_ref[...] + bc_ref[...])
        dya_ref[...] = (dm * sa).astype(BF16)
        dyc_ref[...] = (dm * sc).astype(BF16)
        dga = dm * ya_ref[...] * sa * (1.0 - sa)
        dgc = dm * yc_ref[...] * sc * (1.0 - sc)
        dga_ref[...] = dga.astype(BF16)
        dgc_ref[...] = dgc.astype(BF16)
        pa = jnp.sum(dga, axis=0, keepdims=True)
        pc = jnp.sum(dgc, axis=0, keepdims=True)

        @pl.when(pl.program_id(1) == 0)
        def _():
            dba_ref[...] = pa
            dbc_ref[...] = pc

        @pl.when(pl.program_id(1) > 0)
        def _():
            dba_ref[...] += pa
            dbc_ref[...] += pc

    tile = lambda off: pl.BlockSpec((ROW_TILE, w), lambda j, i: (i, off + j))
    vec = lambda off: pl.BlockSpec((1, w), lambda j, i: (0, off + j))
    return pl.pallas_call(
        body, name="gate_mix_bwd", grid=(nc, S // ROW_TILE),
        in_specs=[tile(0), tile(ga0), tile(gc0), vec(0), vec(nc), tile(0), tile(0)],
        out_specs=[tile(0), tile(0), tile(0), tile(0), vec(0), vec(0)],
        out_shape=[jax.ShapeDtypeStruct((S, D), BF16)] * 4 + [
                   jax.ShapeDtypeStruct((1, D), F32), jax.ShapeDtypeStruct((1, D), F32)],
        compiler_params=_params(("parallel", "arbitrary")),
    )(dmixed, proj, proj, b_gate, b_gate, y_a, y_c)


def _res1_call(x, out, g_post, g_pre):
    S, D = x.shape

    def body(x_ref, o_ref, gp_ref, gq_ref, x1_ref, h2_ref):
        x1 = x_ref[...] + _rms(o_ref[...], gp_ref[...])
        x1_ref[...] = x1
        h2_ref[...] = _rms(x1, gq_ref[...]).astype(BF16)

    return pl.pallas_call(
        body, name="residual_mix", grid=(S // ROW_TILE,),
        in_specs=[_row_spec(D), _row_spec(D), _vec_spec(D), _vec_spec(D)],
        out_specs=[_row_spec(D), _row_spec(D)],
        out_shape=[jax.ShapeDtypeStruct((S, D), F32), jax.ShapeDtypeStruct((S, D), BF16)],
        compiler_params=_params(("parallel",)),
    )(x, out, g_post, g_pre)


def _loss_call(y, x1, g_post, target):
    S, D = y.shape

    def body(y_ref, x1_ref, g_ref, t_ref, loss_ref, dx_ref):
        err = x1_ref[...] + _rms(y_ref[...], g_ref[...]) - t_ref[...]
        dx_ref[...] = err * (1.0 / D)
        part = 0.5 * jnp.sum(jnp.mean(err * err, axis=-1, keepdims=True), axis=0, keepdims=True)
        _accumulate(loss_ref, jnp.broadcast_to(part, (SUBLANES, LANES)))

    return pl.pallas_call(
        body, name="residual_ffn_loss", grid=(S // ROW_TILE,),
        in_specs=[_row_spec(D), _row_spec(D), _vec_spec(D), _row_spec(D)],
        out_specs=[pl.BlockSpec((SUBLANES, LANES), lambda i: (0, 0)), _row_spec(D)],
        out_shape=[jax.ShapeDtypeStruct((SUBLANES, LANES), F32), jax.ShapeDtypeStruct((S, D), F32)],
        compiler_params=_params(("arbitrary",)),
    )(y, x1, g_post, target)


def _rms_bwd_call(x, g, dy, name):
    S, D = x.shape

    def body(x_ref, g_ref, dy_ref, dx_ref, dg_ref):
        dx, dg = _rms_bwd(x_ref[...], g_ref[...], dy_ref[...])
        dx_ref[...] = dx.astype(BF16)
        _accumulate(dg_ref, dg)

    return pl.pallas_call(
        body, name=name, grid=(S // ROW_TILE,),
        in_specs=[_row_spec(D), _vec_spec(D), _row_spec(D)],
        out_specs=[_row_spec(D), _vec_spec(D)],
        out_shape=[jax.ShapeDtypeStruct((S, D), BF16), jax.ShapeDtypeStruct((1, D), F32)],
        compiler_params=_params(("arbitrary",)),
    )(x, g, dy)


def _mid_bwd_call(x1, g_pre, dh2, dx2, out, g_post):
    S, D = x1.shape

    def body(x1_ref, gq_ref, dh_ref, dx2_ref, o_ref, gp_ref, dx1_ref, do_ref, dgq_ref, dgp_ref):
        d, dgq = _rms_bwd(x1_ref[...], gq_ref[...], dh_ref[...])
        dx1 = dx2_ref[...] + d
        dx1_ref[...] = dx1
        do, dgp = _rms_bwd(o_ref[...], gp_ref[...], dx1)
        do_ref[...] = do.astype(BF16)
        _accumulate(dgq_ref, dgq)
        _accumulate(dgp_ref, dgp)

    return pl.pallas_call(
        body, name="residual_mix_bwd", grid=(S // ROW_TILE,),
        in_specs=[_row_spec(D), _vec_spec(D), _row_spec(D), _row_spec(D), _row_spec(D), _vec_spec(D)],
        out_specs=[_row_spec(D), _row_spec(D), _vec_spec(D), _vec_spec(D)],
        out_shape=[jax.ShapeDtypeStruct((S, D), F32), jax.ShapeDtypeStruct((S, D), BF16)] + [jax.ShapeDtypeStruct((1, D), F32)] * 2,
        compiler_params=_params(("arbitrary",)),
    )(x1, g_pre, dh2, dx2, out, g_post)


def _in_bwd_call(x, g, dh1, dx1):
    S, D = x.shape

    def body(x_ref, g_ref, dh_ref, dx1_ref, gx_ref, dg_ref):
        d, dg = _rms_bwd(x_ref[...], g_ref[...], dh_ref[...])
        gx_ref[...] = dx1_ref[...] + d
        _accumulate(dg_ref, dg)

    return pl.pallas_call(
        body, name="rms_mix_pre_bwd", grid=(S // ROW_TILE,),
        in_specs=[_row_spec(D), _vec_spec(D), _row_spec(D), _row_spec(D)],
        out_specs=[_row_spec(D), _vec_spec(D)],
        out_shape=[jax.ShapeDtypeStruct((S, D), F32), jax.ShapeDtypeStruct((1, D), F32)],
        compiler_params=_params(("arbitrary",)),
    )(x, g, dh1, dx1)


def _bucket_table(dilation):
    qi = np.arange(SPAN)[:, None]
    ki = np.arange(2 * SPAN)[None, :]
    dist = np.maximum(qi + SPAN - ki, 0) * dilation
    max_exact = N_BUCKETS // 2
    d = np.maximum(dist, 1).astype(np.float64)
    large = max_exact + (np.log(d / max_exact) / math.log(MAX_DISTANCE / max_exact) * (N_BUCKETS - max_exact)).astype(np.int32)
    large = np.minimum(large, N_BUCKETS - 1)
    return np.where(dist < max_exact, dist, large).astype(np.int32)


def _bucket_tables():
    return jnp.asarray(np.stack([_bucket_table(r) for _, r in DILATED_PATTERNS]))


def _bias_table_call(rel_bias, buckets):
    def body(rb_ref, bk_ref, o_ref):
        for h in range(N_HEADS):
            bk = bk_ref[h // HEADS_PER_GROUP]

            def step(b, acc):
                return jnp.where(bk == b, rb_ref[b, h], acc)

            o_ref[h] = lax.fori_loop(0, N_BUCKETS, step, jnp.zeros((SPAN, 2 * SPAN), F32))

    return pl.pallas_call(
        body, name="rel_bias_table",
        in_specs=[pl.BlockSpec(memory_space=pltpu.SMEM), pl.BlockSpec(memory_space=pltpu.VMEM)],
        out_specs=pl.BlockSpec(memory_space=pltpu.VMEM),
        out_shape=jax.ShapeDtypeStruct((N_HEADS, SPAN, 2 * SPAN), F32),
    )(rel_bias, buckets)


def _bias_grad_call(dbias, buckets):
    def body(db_ref, bk_ref, o_ref, rows_ref):
        for h in range(N_HEADS):
            bk = bk_ref[h // HEADS_PER_GROUP]
            dv = db_ref[h]

            def step(b, carry):
                rows_ref[h, b] = jnp.sum(jnp.where(bk == b, dv, 0.0), axis=0, keepdims=True)
                return carry

            lax.fori_loop(0, N_BUCKETS, step, 0)
        o_ref[...] = jnp.sum(rows_ref[...], axis=-1, keepdims=True)

    out = pl.pallas_call(
        body, name="rel_bias_grad",
        in_specs=[pl.BlockSpec(memory_space=pltpu.VMEM), pl.BlockSpec(memory_space=pltpu.VMEM)],
        out_specs=pl.BlockSpec(memory_space=pltpu.VMEM),
        out_shape=jax.ShapeDtypeStruct((N_HEADS, N_BUCKETS, 1, 1), F32),
        scratch_shapes=[pltpu.VMEM((N_HEADS, N_BUCKETS, 1, 2 * SPAN), F32)],
    )(dbias, buckets)
    return out.reshape(N_HEADS, N_BUCKETS).T


def _dot_nt(a, b):
    return lax.dot_general(a, b, (((1,), (1,)), ((), ())), preferred_element_type=F32)


def _dot_nn(a, b):
    return lax.dot_general(a, b, (((1,), (0,)), ((), ())), preferred_element_type=F32)


def _dot_tn(a, b):
    return lax.dot_general(a, b, (((0,), (0,)), ((), ())), preferred_element_type=F32)


def _band_masks(n, nb):
    qi = lax.broadcasted_iota(jnp.int32, (SPAN, SPAN), 0)
    ki = lax.broadcasted_iota(jnp.int32, (SPAN, SPAN), 1)
    prev_ok = jnp.logical_and(ki >= qi, n > 0)
    cur_ok = ki <= qi
    next_ok = jnp.logical_and(ki >= qi, n < nb - 1)
    return prev_ok, cur_ok, next_ok


def _regroup_call(src, col_block, r, inverse, name):
    S = src.shape[0]
    L = S // r
    nt = GROUP_WIDTH // LANES

    def body(x_ref, o_ref):
        for rho in range(r):
            if inverse:
                o_ref[pl.ds(rho, L, stride=r), :] = x_ref[rho * L:(rho + 1) * L, :]
            else:
                o_ref[rho * L:(rho + 1) * L, :] = x_ref[pl.ds(rho, L, stride=r), :]

    return pl.pallas_call(
        body, name=name, grid=(nt,),
        in_specs=[pl.BlockSpec((S, LANES), lambda i: (0, col_block * nt + i))],
        out_specs=pl.BlockSpec((S, LANES), lambda i: (0, i)),
        out_shape=jax.ShapeDtypeStruct((S, GROUP_WIDTH), F32),
        compiler_params=_params(("parallel",)),
    )(src)


def _to_group_order(src, col_block, group, name):
    r = DILATED_PATTERNS[group][1]
    if r == 1:
        return src, col_block
    return _regroup_call(src, col_block, r, False, name), 0


def _to_token_order(arr, group, name):
    r = DILATED_PATTERNS[group][1]
    return arr if r == 1 else _regroup_call(arr, 0, r, True, name)


def _attn_fwd_call(q, k, v, bias, group):
    S = q[0].shape[0]
    r = DILATED_PATTERNS[group][1]
    nb = S // r // SPAN
    scale = HEAD_DIM ** -0.5

    def body(q_ref, kp_ref, kc_ref, vp_ref, vc_ref, b_ref, o_ref, lse_ref):
        n = pl.program_id(1)
        prev_ok, cur_ok, _ = _band_masks(n, nb)
        for j in range(HEADS_PER_GROUP):
            sl = slice(j * HEAD_DIM, (j + 1) * HEAD_DIM)
            q = q_ref[:, sl].astype(BF16)
            sp = _dot_nt(q, kp_ref[:, sl].astype(BF16)) * scale + b_ref[j, :, :SPAN]
            sc = _dot_nt(q, kc_ref[:, sl].astype(BF16)) * scale + b_ref[j, :, SPAN:]
            sp = jnp.where(prev_ok, sp, NEG_INF)
            sc = jnp.where(cur_ok, sc, NEG_INF)
            m = jnp.maximum(jnp.max(sp, axis=-1, keepdims=True), jnp.max(sc, axis=-1, keepdims=True))
            pp = jnp.exp(sp - m)
            pc = jnp.exp(sc - m)
            den = jnp.sum(pp, axis=-1, keepdims=True) + jnp.sum(pc, axis=-1, keepdims=True)
            o_ref[:, sl] = (_dot_nn(pp.astype(BF16), vp_ref[:, sl].astype(BF16))
                            + _dot_nn(pc.astype(BF16), vc_ref[:, sl].astype(BF16))) / den
            lse_ref[:, sl] = jnp.broadcast_to(m + jnp.log(den), (SPAN, HEAD_DIM))

    blk = (SPAN, GROUP_WIDTH)
    cur = lambda cb: pl.BlockSpec(blk, lambda rho, n: (rho * nb + n, cb))
    prev = lambda cb: pl.BlockSpec(blk, lambda rho, n: (rho * nb + jnp.maximum(n - 1, 0), cb))
    return pl.pallas_call(
        body, name=f"attn_fwd_g{group}", grid=(r, nb),
        in_specs=[cur(q[1]), prev(k[1]), cur(k[1]), prev(v[1]), cur(v[1]),
                  pl.BlockSpec((HEADS_PER_GROUP, SPAN, 2 * SPAN), lambda rho, n: (group, 0, 0))],
        out_specs=[cur(0)] * 2,
        out_shape=[jax.ShapeDtypeStruct((S, GROUP_WIDTH), F32)] * 2,
        compiler_params=_params(("parallel", "parallel")),
    )(q[0], k[0], k[0], v[0], v[0], bias)


def _attn_merge_call(parts):
    S = parts[0].shape[0]

    def body(o1, s1, o2, s2, o3, s3, a_ref, ab_ref, lse_ref):
        mx = jnp.maximum(jnp.maximum(s1[...], s2[...]), s3[...])
        w1 = jnp.exp(s1[...] - mx)
        w2 = jnp.exp(s2[...] - mx)
        w3 = jnp.exp(s3[...] - mx)
        den = w1 + w2 + w3
        a = (w1 * o1[...] + w2 * o2[...] + w3 * o3[...]) / den
        a_ref[...] = a
        ab_ref[...] = a.astype(BF16)
        lse_ref[...] = mx + jnp.log(den)

    return pl.pallas_call(
        body, name="attn_merge", grid=(S // ROW_TILE,),
        in_specs=[_row_spec(GROUP_WIDTH)] * 6, out_specs=[_row_spec(GROUP_WIDTH)] * 3,
        out_shape=[jax.ShapeDtypeStruct((S, GROUP_WIDTH), F32), jax.ShapeDtypeStruct((S, GROUP_WIDTH), BF16),
                   jax.ShapeDtypeStruct((S, GROUP_WIDTH), F32)],
        compiler_params=_params(("parallel",)),
    )(*parts)


def _attn_delta_call(a, da):
    S = a.shape[0]

    def body(a_ref, da_ref, d_ref):
        for j in range(HEADS_PER_GROUP):
            sl = slice(j * HEAD_DIM, (j + 1) * HEAD_DIM)
            d = jnp.sum(a_ref[:, sl] * da_ref[:, sl], axis=-1, keepdims=True)
            d_ref[:, sl] = jnp.broadcast_to(d, (ROW_TILE, HEAD_DIM))

    return pl.pallas_call(
        body, name="attn_delta", grid=(S // ROW_TILE,),
        in_specs=[_row_spec(GROUP_WIDTH)] * 2, out_specs=_row_spec(GROUP_WIDTH),
        out_shape=jax.ShapeDtypeStruct((S, GROUP_WIDTH), F32),
        compiler_params=_params(("parallel",)),
    )(a, da)


def _attn_bwd_call(q, k, v, bias, da, lse, delta, group):
    S = q[0].shape[0]
    r = DILATED_PATTERNS[group][1]
    nb = S // r // SPAN
    scale = HEAD_DIM ** -0.5

    def body(q_ref, qn_ref, kp_ref, kc_ref, vp_ref, vc_ref, b_ref, da_ref, dan_ref, lse_ref, lsen_ref, dl_ref, dln_ref,
             dq_ref, dk_ref, dv_ref, db_ref):
        n = pl.program_id(1)
        prev_ok, cur_ok, next_ok = _band_masks(n, nb)
        first = jnp.logical_and(pl.program_id(0) == 0, n == 0)
        for j in range(HEADS_PER_GROUP):
            sl = slice(j * HEAD_DIM, (j + 1) * HEAD_DIM)
            q = q_ref[:, sl].astype(BF16)
            qn = qn_ref[:, sl].astype(BF16)
            kp = kp_ref[:, sl].astype(BF16)
            kc = kc_ref[:, sl].astype(BF16)
            vp = vp_ref[:, sl].astype(BF16)
            vc = vc_ref[:, sl].astype(BF16)
            dav = da_ref[:, sl].astype(BF16)
            dan = dan_ref[:, sl].astype(BF16)
            bp = b_ref[j, :, :SPAN]
            bc = b_ref[j, :, SPAN:]
            pp = jnp.exp(jnp.where(prev_ok, _dot_nt(q, kp) * scale + bp, NEG_INF) - lse_ref[:, sl])
            pc = jnp.exp(jnp.where(cur_ok, _dot_nt(q, kc) * scale + bc, NEG_INF) - lse_ref[:, sl])
            pn = jnp.exp(jnp.where(next_ok, _dot_nt(qn, kc) * scale + bp, NEG_INF) - lsen_ref[:, sl])
            dsp = pp * (_dot_nt(dav, vp) - dl_ref[:, sl])
            dsc = pc * (_dot_nt(dav, vc) - dl_ref[:, sl])
            dsn = pn * (_dot_nt(dan, vc) - dln_ref[:, sl])
            dsp_b, dsc_b, dsn_b = dsp.astype(BF16), dsc.astype(BF16), dsn.astype(BF16)
            dq_ref[:, sl] = (_dot_nn(dsp_b, kp) + _dot_nn(dsc_b, kc)) * scale
            dk_ref[:, sl] = (_dot_tn(dsc_b, q) + _dot_tn(dsn_b, qn)) * scale
            dv_ref[:, sl] = _dot_tn(pc.astype(BF16), dav) + _dot_tn(pn.astype(BF16), dan)

            @pl.when(first)
            def _():
                db_ref[j, :, :SPAN] = dsp
                db_ref[j, :, SPAN:] = dsc

            @pl.when(jnp.logical_not(first))
            def _():
                db_ref[j, :, :SPAN] += dsp
                db_ref[j, :, SPAN:] += dsc

    blk = (SPAN, GROUP_WIDTH)
    cur = lambda cb: pl.BlockSpec(blk, lambda rho, n: (rho * nb + n, cb))
    prev = lambda cb: pl.BlockSpec(blk, lambda rho, n: (rho * nb + jnp.maximum(n - 1, 0), cb))
    nxt = lambda cb: pl.BlockSpec(blk, lambda rho, n: (rho * nb + jnp.minimum(n + 1, nb - 1), cb))
    band = (HEADS_PER_GROUP, SPAN, 2 * SPAN)
    return pl.pallas_call(
        body, name=f"attn_bwd_g{group}", grid=(r, nb),
        in_specs=[cur(q[1]), nxt(q[1]), prev(k[1]), cur(k[1]), prev(v[1]), cur(v[1]),
                  pl.BlockSpec(band, lambda rho, n: (group, 0, 0)),
                  cur(0), nxt(0), cur(0), nxt(0), cur(0), nxt(0)],
        out_specs=[cur(0), cur(0), cur(0), pl.BlockSpec(band, lambda rho, n: (0, 0, 0))],
        out_shape=[jax.ShapeDtypeStruct((S, GROUP_WIDTH), F32)] * 3 + [jax.ShapeDtypeStruct(band, F32)],
        compiler_params=_params(("arbitrary", "arbitrary")),
    )(q[0], q[0], k[0], k[0], v[0], v[0], bias, da, da, lse, lse, delta, delta)


def _taps(width):
    return [(k, (width - 1 - k) // SUBLANES, (width - 1 - k) % SUBLANES) for k in range(width)]


def _shifted(win, width, pad, up):
    total = win.shape[0]
    for b in range(SUBLANES):
        taps = [(k, a) for k, a, bb in _taps(width) if bb == b]
        if not taps:
            continue
        if up:
            rolled = win if b == 0 else pltpu.roll(win, total - b, axis=0)
        else:
            rolled = win if b == 0 else pltpu.roll(win, b, axis=0)
        for k, a in taps:
            start = SUBLANES * a if up else pad - SUBLANES * a
            yield k, rolled[start:start + TIME_BLOCK, :]


def _conv_block(win, w_ref, width, pad):
    acc = None
    for k, rows in _shifted(win, width, pad, up=False):
        term = w_ref[k:k + 1, :] * rows
        acc = term if acc is None else acc + term
    return acc


def _conv_transpose_block(win, w_ref, width, pad):
    acc = None
    for k, rows in _shifted(win, width, pad, up=True):
        term = w_ref[k:k + 1, :] * rows
        acc = term if acc is None else acc + term
    return acc


def _conv_weight_grad(win, dy, dw_ref, width, pad):
    for k, rows in _shifted(win, width, pad, up=False):
        dw_ref[k:k + 1, :] += jnp.sum(dy * rows, axis=0, keepdims=True)


def _time_loop(S, step):
    def it(tb, carry):
        step(pl.multiple_of(tb * TIME_BLOCK, TIME_BLOCK))
        return carry

    lax.fori_loop(0, S // TIME_BLOCK, it, 0)


def _conv_fwd_call(proj, col0, w, b):
    S = proj.shape[0]
    C = w.shape[1]
    nt = C // LANES
    v0, g0 = col0 // LANES, (col0 + C) // LANES

    def body(val_ref, gate_ref, w_ref, b_ref, o_ref, pad_ref):
        pad_ref[0:CONV_PAD, :] = jnp.zeros((CONV_PAD, LANES), F32)
        pad_ref[CONV_PAD:, :] = val_ref[...] * _sigmoid(gate_ref[...])

        def step(t0):
            win = pad_ref[pl.ds(t0, TIME_BLOCK + CONV_PAD), :]
            o_ref[pl.ds(t0, TIME_BLOCK), :] = _conv_block(win, w_ref, CONV_WIDTH, CONV_PAD) + b_ref[...]

        _time_loop(S, step)

    seq = lambda off: pl.BlockSpec((S, LANES), lambda i: (0, off + i))
    return pl.pallas_call(
        body, name="conv_module", grid=(nt,),
        in_specs=[seq(v0), seq(g0), pl.BlockSpec((CONV_WIDTH, LANES), lambda i: (0, i)), pl.BlockSpec((1, LANES), lambda i: (0, i))],
        out_specs=seq(0), out_shape=jax.ShapeDtypeStruct((S, C), F32),
        scratch_shapes=[pltpu.VMEM((S + CONV_PAD, LANES), F32)],
        compiler_params=_params(("parallel",)),
    )(proj, proj, w, b)


def _conv_bwd_call(proj, col0, w, dc1):
    S = proj.shape[0]
    C = w.shape[1]
    nt = C // LANES
    v0, g0 = col0 // LANES, (col0 + C) // LANES

    def body(val_ref, gate_ref, w_ref, dy_ref, dval_ref, dgate_ref, dw_ref, db_ref, xpad_ref, dpad_ref, dwacc_ref):
        xpad_ref[0:CONV_PAD, :] = jnp.zeros((CONV_PAD, LANES), F32)
        xpad_ref[CONV_PAD:, :] = val_ref[...] * _sigmoid(gate_ref[...])
        dpad_ref[0:S, :] = dy_ref[...]
        dpad_ref[S:, :] = jnp.zeros((CONV_PAD, LANES), F32)
        dwacc_ref[...] = jnp.zeros_like(dwacc_ref)

        def step(t0):
            rows = pl.ds(t0, TIME_BLOCK)
            _conv_weight_grad(xpad_ref[pl.ds(t0, TIME_BLOCK + CONV_PAD), :], dy_ref[rows, :], dwacc_ref, CONV_WIDTH, CONV_PAD)
            dc0 = _conv_transpose_block(dpad_ref[pl.ds(t0, TIME_BLOCK + CONV_PAD), :], w_ref, CONV_WIDTH, CONV_PAD)
            sg = _sigmoid(gate_ref[rows, :])
            dval_ref[rows, :] = (dc0 * sg).astype(BF16)
            dgate_ref[rows, :] = (dc0 * val_ref[rows, :] * sg * (1.0 - sg)).astype(BF16)

        _time_loop(S, step)
        dw_ref[...] = dwacc_ref[...]
        db_ref[...] = jnp.sum(dy_ref[...], axis=0, keepdims=True)

    seq = lambda off: pl.BlockSpec((S, LANES), lambda i: (0, off + i))
    return pl.pallas_call(
        body, name="conv_module_bwd", grid=(nt,),
        in_specs=[seq(v0), seq(g0), pl.BlockSpec((CONV_WIDTH, LANES), lambda i: (0, i)), seq(0)],
        out_specs=[seq(0), seq(0), pl.BlockSpec((CONV_PAD, LANES), lambda i: (0, i)), pl.BlockSpec((1, LANES), lambda i: (0, i))],
        out_shape=[jax.ShapeDtypeStruct((S, C), BF16), jax.ShapeDtypeStruct((S, C), BF16),
                   jax.ShapeDtypeStruct((CONV_PAD, C), F32), jax.ShapeDtypeStruct((1, C), F32)],
        scratch_shapes=[pltpu.VMEM((S + CONV_PAD, LANES), F32), pltpu.VMEM((S + CONV_PAD, LANES), F32),
                        pltpu.VMEM((CONV_PAD, LANES), F32)],
        compiler_params=_params(("parallel",)),
    )(proj, proj, w, dc1)


def _ffn_fwd_call(u, w, b):
    S, C2 = u.shape
    C = C2 // 2
    nt = C // LANES

    def body(ug_ref, uv_ref, wg_ref, wv_ref, bg_ref, bv_ref, f_ref, pg_ref, pv_ref):
        zeros = jnp.zeros((FFN_PAD, LANES), F32)
        pg_ref[0:FFN_PAD, :] = zeros
        pv_ref[0:FFN_PAD, :] = zeros
        pg_ref[FFN_PAD:, :] = ug_ref[...]
        pv_ref[FFN_PAD:, :] = uv_ref[...]

        def step(t0):
            win = pl.ds(t0, TIME_BLOCK + FFN_PAD)
            cg = _conv_block(pg_ref[win, :], wg_ref, FFN_CONV_WIDTH, FFN_PAD) + bg_ref[...]
            cv = _conv_block(pv_ref[win, :], wv_ref, FFN_CONV_WIDTH, FFN_PAD) + bv_ref[...]
            f_ref[pl.ds(t0, TIME_BLOCK), :] = (_gelu(cg) * cv).astype(BF16)

        _time_loop(S, step)

    seq = lambda off: pl.BlockSpec((S, LANES), lambda i: (0, off + i))
    wsp = lambda off: pl.BlockSpec((FFN_CONV_WIDTH, LANES), lambda i: (0, off + i))
    bsp = lambda off: pl.BlockSpec((1, LANES), lambda i: (0, off + i))
    return pl.pallas_call(
        body, name="ffn_conv_geglu", grid=(nt,),
        in_specs=[seq(0), seq(nt), wsp(0), wsp(nt), bsp(0), bsp(nt)],
        out_specs=seq(0), out_shape=jax.ShapeDtypeStruct((S, C), BF16),
        scratch_shapes=[pltpu.VMEM((S + FFN_PAD, LANES), F32)] * 2,
        compiler_params=_params(("parallel",)),
    )(u, u, w, w, b, b)


def _ffn_bwd_call(u, w, b, df):
    S, C2 = u.shape
    C = C2 // 2
    nt = C // LANES

    def body(ug_ref, uv_ref, wg_ref, wv_ref, bg_ref, bv_ref, df_ref,
             dug_ref, duv_ref, dwg_ref, dwv_ref, dbg_ref, dbv_ref,
             pg_ref, pv_ref, dg_ref, dv_ref, dwg_acc, dwv_acc, dbg_acc, dbv_acc):
        zeros = jnp.zeros((FFN_PAD, LANES), F32)
        pg_ref[0:FFN_PAD, :] = zeros
        pv_ref[0:FFN_PAD, :] = zeros
        pg_ref[FFN_PAD:, :] = ug_ref[...]
        pv_ref[FFN_PAD:, :] = uv_ref[...]
        dg_ref[S:, :] = zeros
        dv_ref[S:, :] = zeros
        dwg_acc[...] = jnp.zeros_like(dwg_acc)
        dwv_acc[...] = jnp.zeros_like(dwv_acc)
        dbg_acc[...] = jnp.zeros_like(dbg_acc)
        dbv_acc[...] = jnp.zeros_like(dbv_acc)

        def first(t0):
            win = pl.ds(t0, TIME_BLOCK + FFN_PAD)
            rows = pl.ds(t0, TIME_BLOCK)
            xg = pg_ref[win, :]
            xv = pv_ref[win, :]
            cg = _conv_block(xg, wg_ref, FFN_CONV_WIDTH, FFN_PAD) + bg_ref[...]
            cv = _conv_block(xv, wv_ref, FFN_CONV_WIDTH, FFN_PAD) + bv_ref[...]
            dfb = df_ref[rows, :]
            dcg = dfb * cv * _gelu_grad(cg)
            dcv = dfb * _gelu(cg)
            dg_ref[rows, :] = dcg
            dv_ref[rows, :] = dcv
            _conv_weight_grad(xg, dcg, dwg_acc, FFN_CONV_WIDTH, FFN_PAD)
            _conv_weight_grad(xv, dcv, dwv_acc, FFN_CONV_WIDTH, FFN_PAD)
            dbg_acc[...] += jnp.sum(dcg, axis=0, keepdims=True)
            dbv_acc[...] += jnp.sum(dcv, axis=0, keepdims=True)

        def second(t0):
            win = pl.ds(t0, TIME_BLOCK + FFN_PAD)
            rows = pl.ds(t0, TIME_BLOCK)
            dug_ref[rows, :] = _conv_transpose_block(dg_ref[win, :], wg_ref, FFN_CONV_WIDTH, FFN_PAD).astype(BF16)
            duv_ref[rows, :] = _conv_transpose_block(dv_ref[win, :], wv_ref, FFN_CONV_WIDTH, FFN_PAD).astype(BF16)

        _time_loop(S, first)
        _time_loop(S, second)
        dwg_ref[...] = dwg_acc[...]
        dwv_ref[...] = dwv_acc[...]
        dbg_ref[...] = dbg_acc[...]
        dbv_ref[...] = dbv_acc[...]

    seq = lambda off: pl.BlockSpec((S, LANES), lambda i: (0, off + i))
    wsp = lambda off: pl.BlockSpec((FFN_CONV_WIDTH, LANES), lambda i: (0, off + i))
    bsp = lambda off: pl.BlockSpec((1, LANES), lambda i: (0, off + i))
    return pl.pallas_call(
        body, name="ffn_conv_geglu_bwd", grid=(nt,),
        in_specs=[seq(0), seq(nt), wsp(0), wsp(nt), bsp(0), bsp(nt), seq(0)],
        out_specs=[seq(0), seq(0), pl.BlockSpec((SUBLANES, LANES), lambda i: (0, i)), pl.BlockSpec((SUBLANES, LANES), lambda i: (0, i)),
                   bsp(0), bsp(0)],
        out_shape=[jax.ShapeDtypeStruct((S, C), BF16)] * 2 + [jax.ShapeDtypeStruct((SUBLANES, C), F32)] * 2
        + [jax.ShapeDtypeStruct((1, C), F32)] * 2,
        scratch_shapes=[pltpu.VMEM((S + FFN_PAD, LANES), F32)] * 4 + [pltpu.VMEM((SUBLANES, LANES), F32)] * 2
        + [pltpu.VMEM((1, LANES), F32)] * 2,
        compiler_params=_params(("parallel",)),
    )(u, u, w, w, b, b, df)


def _adamw_call(w, g, m, v, name):
    R, C = w.shape
    tr = _row_tile(R, C)
    c1 = 1.0 / (1.0 - ADAM_B1 ** ADAM_STEP)
    c2 = 1.0 / (1.0 - ADAM_B2 ** ADAM_STEP)

    def body(w_ref, g_ref, m_ref, v_ref, d_ref, mo_ref, vo_ref):
        gv = g_ref[...]
        mn = ADAM_B1 * m_ref[...] + (1.0 - ADAM_B1) * gv
        vn = ADAM_B2 * v_ref[...] + (1.0 - ADAM_B2) * (gv * gv)
        mo_ref[...] = mn
        vo_ref[...] = vn
        d_ref[...] = -ADAM_LR * ((mn * c1) / (jnp.sqrt(vn * c2) + ADAM_EPS) + ADAM_WD * w_ref[...])

    spec = pl.BlockSpec((tr, C), lambda i: (i, 0))
    return pl.pallas_call(
        body, name=name, grid=(R // tr,),
        in_specs=[spec] * 4, out_specs=[spec] * 3,
        out_shape=[jax.ShapeDtypeStruct((R, C), F32)] * 3,
        compiler_params=_params(("parallel",)),
    )(w, g, m, v)


def _position():
    return lax.axis_index("x"), lax.axis_index("y"), lax.axis_index("c")


def _chip_peers(x, y):
    return [(x, 1 - y), (1 - x, y), (1 - x, 1 - y)]


def _half_rows(ref, core, rows):
    h = rows // 2
    start = pl.multiple_of(core * h, 16)
    return ref.at[pl.ds(start, h), :] if len(ref.shape) == 2 else ref.at[:, pl.ds(start, h), :]


def _shard_half(ref, shard, core, rows):
    h = rows // 2
    return ref.at[shard, pl.ds(pl.multiple_of(core * h, 16), h), :]


ANY = pl.BlockSpec(memory_space=pl.ANY)


def _allgather_call(shards, whole):
    n, nw = len(shards), len(whole)
    outs_shape = [jax.ShapeDtypeStruct((N_CHIPS,) + s.shape, s.dtype) for s in shards + whole]

    def body(*refs):
        ins, outs = refs[:n + nw], refs[n + nw:2 * (n + nw)]
        send_sems, recv_sems, pass_send, pass_recv, local_sems = refs[2 * (n + nw):]
        x, y, c = _position()
        chip = 2 * x + y
        peers = _chip_peers(x, y)
        sent, local = [], []
        for i in range(n + nw):
            cp = pltpu.make_async_copy(ins[i], outs[i].at[chip], local_sems.at[i])
            cp.start()
            local.append(cp)
            rows = ins[i].shape[0]
            for k, (px, py) in enumerate(peers):
                if i < n:
                    src, dst = _half_rows(ins[i], c, rows), _shard_half(outs[i], chip, c, rows)
                else:
                    src, dst = ins[i], outs[i].at[chip]
                cp = pltpu.make_async_remote_copy(src_ref=src, dst_ref=dst, send_sem=send_sems.at[i, k],
                                                  recv_sem=recv_sems.at[i, k], device_id=(px, py, c), device_id_type=MESH)
                cp.start()
                sent.append(cp)
        passed = []
        for i in range(n + nw):
            rows = ins[i].shape[0]
            for k, (px, py) in enumerate(peers):
                landed = _shard_half(outs[i], 2 * px + py, c, rows) if i < n else outs[i].at[2 * px + py]
                pltpu.make_async_remote_copy(src_ref=landed, dst_ref=landed, send_sem=send_sems.at[i, k],
                                             recv_sem=recv_sems.at[i, k], device_id=(px, py, c), device_id_type=MESH).wait_recv()
                if i < n:
                    cp = pltpu.make_async_remote_copy(src_ref=landed, dst_ref=landed, send_sem=pass_send.at[i, k],
                                                      recv_sem=pass_recv.at[i, k], device_id=(x, y, 1 - c), device_id_type=MESH)
                    cp.start()
                    passed.append(cp)
        for cp in sent:
            cp.wait_send()
        for cp in passed:
            cp.wait()
        for cp in local:
            cp.wait()

    return pl.pallas_call(
        body, name="weight_allgather",
        in_specs=[ANY] * (n + nw), out_specs=[ANY] * (n + nw), out_shape=outs_shape,
        scratch_shapes=[pltpu.SemaphoreType.DMA((n + nw, 3)), pltpu.SemaphoreType.DMA((n + nw, 3)),
                        pltpu.SemaphoreType.DMA((n, 3)), pltpu.SemaphoreType.DMA((n, 3)), pltpu.SemaphoreType.DMA((n + nw,))],
    )(*shards, *whole)


def _sibling_exchange_call(grads):
    n = len(grads)
    halves = [jax.ShapeDtypeStruct((N_CHIPS, g.shape[1] // 2, g.shape[2]), F32) for g in grads]

    def body(*refs):
        ins, outs = refs[:n], refs[n:2 * n]
        send_sems, recv_sems = refs[2 * n:]
        x, y, c = _position()
        copies = []
        for i in range(n):
            cp = pltpu.make_async_remote_copy(src_ref=_half_rows(ins[i], 1 - c, grads[i].shape[1]), dst_ref=outs[i],
                                              send_sem=send_sems.at[i], recv_sem=recv_sems.at[i],
                                              device_id=(x, y, 1 - c), device_id_type=MESH)
            cp.start()
            copies.append(cp)
        for cp in copies:
            cp.wait()

    return pl.pallas_call(
        body, name="grad_sibling_exchange",
        in_specs=[ANY] * n, out_specs=[ANY] * n, out_shape=halves,
        scratch_shapes=[pltpu.SemaphoreType.DMA((n,)), pltpu.SemaphoreType.DMA((n,))],
    )(*grads)


def _pair_sum_call(grad, recv, core, name):
    _, h, B = recv.shape
    tr = _row_tile(h, B)

    def body(core_ref, g_ref, r_ref, o_ref, ob_ref):
        s = g_ref[...] + r_ref[...]
        o_ref[...] = s
        ob_ref[...] = s.astype(BF16)

    g_spec = pl.BlockSpec((None, tr, B), lambda q, i, core_ref: (q, core_ref[0] * (h // tr) + i, 0))
    spec = pl.BlockSpec((None, tr, B), lambda q, i, core_ref: (q, i, 0))
    return pl.pallas_call(
        body, name=name,
        grid_spec=pltpu.PrefetchScalarGridSpec(num_scalar_prefetch=1, grid=(N_CHIPS, h // tr), in_specs=[g_spec, spec],
                                               out_specs=[spec, spec]),
        out_shape=[jax.ShapeDtypeStruct(recv.shape, F32), jax.ShapeDtypeStruct(recv.shape, BF16)],
        compiler_params=_params(("parallel", "parallel")),
    )(core, grad, recv)


def _chip_exchange_call(partials):
    n = len(partials)
    outs_shape = [jax.ShapeDtypeStruct((3,) + t.shape[1:], BF16) for t in partials]

    def body(*refs):
        ins, outs = refs[:n], refs[n:2 * n]
        send_sems, recv_sems = refs[2 * n:]
        x, y, c = _position()
        copies = []
        for i in range(n):
            for k, (px, py) in enumerate(_chip_peers(x, y)):
                cp = pltpu.make_async_remote_copy(src_ref=ins[i].at[2 * px + py], dst_ref=outs[i].at[k], send_sem=send_sems.at[i, k],
                                                  recv_sem=recv_sems.at[i, k], device_id=(px, py, c), device_id_type=MESH)
                cp.start()
                copies.append(cp)
        for cp in copies:
            cp.wait()

    return pl.pallas_call(
        body, name="grad_chip_exchange",
        in_specs=[ANY] * n, out_specs=[ANY] * n, out_shape=outs_shape,
        scratch_shapes=[pltpu.SemaphoreType.DMA((n, 3)), pltpu.SemaphoreType.DMA((n, 3))],
    )(*partials)


def _chip_sum_call(partial, recv, chip, name):
    _, h, B = recv.shape
    tr = _row_tile(h, B)

    def body(chip_ref, p_ref, r_ref, o_ref):
        o_ref[...] = ((p_ref[...] + r_ref[0].astype(F32)) + r_ref[1].astype(F32)) + r_ref[2].astype(F32)

    return pl.pallas_call(
        body, name=name,
        grid_spec=pltpu.PrefetchScalarGridSpec(
            num_scalar_prefetch=1, grid=(h // tr,),
            in_specs=[pl.BlockSpec((None, tr, B), lambda i, chip_ref: (chip_ref[0], i, 0)),
                      pl.BlockSpec((3, tr, B), lambda i, chip_ref: (0, i, 0))],
            out_specs=pl.BlockSpec((tr, B), lambda i, chip_ref: (i, 0))),
        out_shape=jax.ShapeDtypeStruct((h, B), F32),
        compiler_params=_params(("parallel",)),
    )(chip, partial, recv)


def _sibling_assemble_call(halves):
    n = len(halves)
    shards = [jax.ShapeDtypeStruct((h.shape[0] * 2, h.shape[1]), F32) for h in halves]

    def body(*refs):
        ins, outs = refs[:n], refs[n:2 * n]
        send_sems, recv_sems, local_sems = refs[2 * n:]
        x, y, c = _position()
        copies = []
        for i in range(n):
            place = _half_rows(outs[i], c, 2 * halves[i].shape[0])
            local = pltpu.make_async_copy(ins[i], place, local_sems.at[i])
            local.start()
            copies.append(local)
            cp = pltpu.make_async_remote_copy(src_ref=ins[i], dst_ref=place, send_sem=send_sems.at[i], recv_sem=recv_sems.at[i],
                                              device_id=(x, y, 1 - c), device_id_type=MESH)
            cp.start()
            copies.append(cp)
        for cp in copies:
            cp.wait()

    return pl.pallas_call(
        body, name="grad_sibling_assemble",
        in_specs=[ANY] * n, out_specs=[ANY] * n, out_shape=shards,
        scratch_shapes=[pltpu.SemaphoreType.DMA((n,)), pltpu.SemaphoreType.DMA((n,)), pltpu.SemaphoreType.DMA((n,))],
    )(*halves)


def _small_allreduce_call(packed):
    rows = packed.shape[0]

    def body(x_ref, o_ref, buf_ref, send_sems, recv_sems):
        x, y, c = _position()
        me = 4 * x + 2 * y + c
        buf_ref[me] = x_ref[...]
        copies = []
        for k in range(1, 8):
            peer = (1 - x if k & 4 else x, 1 - y if k & 2 else y, 1 - c if k & 1 else c)
            cp = pltpu.make_async_remote_copy(src_ref=buf_ref.at[me], dst_ref=buf_ref.at[me], send_sem=send_sems.at[k - 1],
                                              recv_sem=recv_sems.at[k - 1], device_id=peer, device_id_type=MESH)
            cp.start()
            copies.append(cp)
        for cp in copies:
            cp.wait()
        acc = buf_ref[0]
        for d in range(1, 8):
            acc = acc + buf_ref[d]
        o_ref[...] = acc

    return pl.pallas_call(
        body, name="small_grad_allreduce",
        in_specs=[pl.BlockSpec(memory_space=pltpu.VMEM)], out_specs=pl.BlockSpec(memory_space=pltpu.VMEM),
        out_shape=jax.ShapeDtypeStruct((rows, LANES), F32),
        scratch_shapes=[pltpu.VMEM((8, rows, LANES), F32), pltpu.SemaphoreType.DMA((7,)), pltpu.SemaphoreType.DMA((7,))],
    )(packed)


def _pack(arrays):
    flat = jnp.concatenate([a.reshape(-1).astype(F32) for a in arrays])
    rows = -(-flat.shape[0] // LANES)
    rows = -(-rows // SUBLANES) * SUBLANES
    flat = jnp.pad(flat, (0, rows * LANES - flat.shape[0]))
    return flat.reshape(rows, LANES)


def _unpack(packed, shapes):
    flat = packed.reshape(-1)
    out, off = [], 0
    for shp in shapes:
        size = int(np.prod(shp))
        out.append(flat[off:off + size].reshape(shp))
        off += size
    return out


def _local_step(xs, target, P):
    S, D = xs.shape
    qkv_width = 3 * N_HEADS * HEAD_DIM
    glu_col0, gate_col0 = qkv_width, qkv_width + 2 * D
    shard_major = lambda g: g.reshape(N_CHIPS, g.shape[0] // N_CHIPS, g.shape[1])

    h1 = _rms_fwd_call(xs, P["norm_mix_pre"])
    proj = _matmul(h1, P["w_in"], "nn", "proj_in")
    buckets = _bucket_tables()
    bias = _bias_table_call(P["rel_bias"], buckets)
    qkv, parts = [], []
    for g in range(N_GROUPS):
        q, k, v = [_to_group_order(proj, 3 * t + g, g, f"group_order_{'qkv'[t]}{g}") for t in range(3)]
        qkv.append((q, k, v))
        o_g, lse_g = _attn_fwd_call(q, k, v, bias, g)
        parts += [_to_token_order(o_g, g, f"token_order_o{g}"), _to_token_order(lse_g, g, f"token_order_lse{g}")]
    a, a_bf, lse = _attn_merge_call(parts)
    y_a = _matmul(a_bf, P["w_attn_out"], "nn", "attn_out")
    c1 = _conv_fwd_call(proj, glu_col0, P["conv_dw_w"], P["conv_dw_b"])
    cact = _ln_silu_call(c1, P["conv_ln_g"], P["conv_ln_b"])
    y_c = _matmul(cact, P["conv_pw_w"], "nn", "conv_pw")
    mixed = _mix_call(proj, gate_col0, P["b_gate"], y_a, y_c)
    out = _matmul(mixed, P["w_out"], "nn", "mix_out")
    x1, h2 = _res1_call(xs, out, P["norm_mix_post"], P["norm_ffn_pre"])
    u = _matmul(h2, P["w_up"], "nn", "ffn_up")
    f = _ffn_fwd_call(u, P["ffn_conv_w"], P["ffn_conv_b"])
    yff = _matmul(f, P["w_down"], "nn", "ffn_down")
    loss_tile, dx2 = _loss_call(yff, x1, P["norm_ffn_post"], target)

    G = {}
    dyff, G["norm_ffn_post"] = _rms_bwd_call(yff, P["norm_ffn_post"], dx2, "rms_ffn_post_bwd")
    df = _matmul(dyff, P["w_down"], "nt", "ffn_down_dx")
    G["w_down"] = shard_major(_matmul(f, dyff, "tn", "ffn_down_dw"))
    dug, duv, dwg, dwv, dbg, dbv = _ffn_bwd_call(u, P["ffn_conv_w"], P["ffn_conv_b"], df)
    G["ffn_conv_w"] = jnp.concatenate([dwg[:FFN_CONV_WIDTH], dwv[:FFN_CONV_WIDTH]], axis=1)
    G["ffn_conv_b"] = jnp.concatenate([dbg, dbv], axis=1)
    du = jnp.concatenate([dug, duv], axis=1)
    dh2 = _matmul(du, P["w_up"], "nt", "ffn_up_dx")
    G["w_up"] = _matmul(h2, du, "tn", "ffn_up_dw", out_shards=True)
    dx1, dout, G["norm_ffn_pre"], G["norm_mix_post"] = _mid_bwd_call(x1, P["norm_ffn_pre"], dh2, dx2, out, P["norm_mix_post"])
    dmixed = _matmul(dout, P["w_out"], "nt", "mix_out_dx")
    G["w_out"] = shard_major(_matmul(mixed, dout, "tn", "mix_out_dw"))
    dya, dyc, dga, dgc, dba, dbc = _mix_bwd_call(dmixed, proj, gate_col0, P["b_gate"], y_a, y_c)
    G["b_gate"] = jnp.concatenate([dba, dbc], axis=1)
    da = _matmul(dya, P["w_attn_out"], "nt", "attn_out_dx")
    G["w_attn_out"] = _matmul(a_bf, dya, "tn", "attn_out_dw", out_shards=True)
    dcact = _matmul(dyc, P["conv_pw_w"], "nt", "conv_pw_dx")
    G["conv_pw_w"] = shard_major(_matmul(cact, dyc, "tn", "conv_pw_dw"))
    dc1, G["conv_ln_g"], G["conv_ln_b"] = _ln_silu_bwd_call(c1, P["conv_ln_g"], P["conv_ln_b"], dcact)
    dval, dgate, dw_dw, G["conv_dw_b"] = _conv_bwd_call(proj, glu_col0, P["conv_dw_w"], dc1)
    G["conv_dw_w"] = dw_dw[:CONV_WIDTH]
    delta = _attn_delta_call(a, da)
    dqs, dks, dvs, dbs = [], [], [], []
    for g in range(N_GROUPS):
        grouped = [_to_group_order(t, 0, g, f"group_order_{n}{g}")[0] for t, n in ((da, "da"), (lse, "lse"), (delta, "delta"))]
        dq, dk, dv, db = _attn_bwd_call(*qkv[g], bias, *grouped, g)
        dqs.append(_to_token_order(dq, g, f"token_order_dq{g}"))
        dks.append(_to_token_order(dk, g, f"token_order_dk{g}"))
        dvs.append(_to_token_order(dv, g, f"token_order_dv{g}"))
        dbs.append(db)
    G["rel_bias"] = _bias_grad_call(jnp.concatenate(dbs, axis=0), buckets)
    dproj = jnp.concatenate([t.astype(BF16) for t in dqs + dks + dvs] + [dval, dgate, dga, dgc], axis=1)
    dh1 = _matmul(dproj, P["w_in"], "nt", "proj_in_dx")
    G["w_in"] = _matmul(h1, dproj, "tn", "proj_in_dw", out_shards=True, tm=512)
    grad_x, G["norm_mix_pre"] = _in_bwd_call(xs, P["norm_mix_pre"], dh1, dx1)
    return loss_tile, grad_x, G


def kernel(x, w_in, b_gate, rel_bias, w_attn_out, conv_dw_w, conv_dw_b, conv_ln_g, conv_ln_b, conv_pw_w, w_out, norm_mix_pre, norm_mix_post, norm_ffn_pre, norm_ffn_post, w_up, ffn_conv_w, ffn_conv_b, w_down, loss_target, m_w_in, m_b_gate, m_rel_bias, m_w_attn_out, m_conv_dw_w, m_conv_dw_b, m_conv_ln_g, m_conv_ln_b, m_conv_pw_w, m_w_out, m_norm_mix_pre, m_norm_mix_post, m_norm_ffn_pre, m_norm_ffn_post, m_w_up, m_ffn_conv_w, m_ffn_conv_b, m_w_down, v_w_in, v_b_gate, v_rel_bias, v_w_attn_out, v_conv_dw_w, v_conv_dw_b, v_conv_ln_g, v_conv_ln_b, v_conv_pw_w, v_w_out, v_norm_mix_pre, v_norm_mix_post, v_norm_ffn_pre, v_norm_ffn_post, v_w_up, v_ffn_conv_w, v_ffn_conv_b, v_w_down):
    weights = dict(w_in=w_in, b_gate=b_gate, rel_bias=rel_bias, w_attn_out=w_attn_out, conv_dw_w=conv_dw_w, conv_dw_b=conv_dw_b,
                   conv_ln_g=conv_ln_g, conv_ln_b=conv_ln_b, conv_pw_w=conv_pw_w, w_out=w_out, norm_mix_pre=norm_mix_pre,
                   norm_mix_post=norm_mix_post, norm_ffn_pre=norm_ffn_pre, norm_ffn_post=norm_ffn_post, w_up=w_up,
                   ffn_conv_w=ffn_conv_w, ffn_conv_b=ffn_conv_b, w_down=w_down)
    m_in = dict(w_in=m_w_in, b_gate=m_b_gate, rel_bias=m_rel_bias, w_attn_out=m_w_attn_out, conv_dw_w=m_conv_dw_w,
                conv_dw_b=m_conv_dw_b, conv_ln_g=m_conv_ln_g, conv_ln_b=m_conv_ln_b, conv_pw_w=m_conv_pw_w, w_out=m_w_out,
                norm_mix_pre=m_norm_mix_pre, norm_mix_post=m_norm_mix_post, norm_ffn_pre=m_norm_ffn_pre,
                norm_ffn_post=m_norm_ffn_post, w_up=m_w_up, ffn_conv_w=m_ffn_conv_w, ffn_conv_b=m_ffn_conv_b, w_down=m_w_down)
    v_in = dict(w_in=v_w_in, b_gate=v_b_gate, rel_bias=v_rel_bias, w_attn_out=v_w_attn_out, conv_dw_w=v_conv_dw_w,
                conv_dw_b=v_conv_dw_b, conv_ln_g=v_conv_ln_g, conv_ln_b=v_conv_ln_b, conv_pw_w=v_conv_pw_w, w_out=v_w_out,
                norm_mix_pre=v_norm_mix_pre, norm_mix_post=v_norm_mix_post, norm_ffn_pre=v_norm_ffn_pre,
                norm_ffn_post=v_norm_ffn_post, w_up=v_w_up, ffn_conv_w=v_ffn_conv_w, ffn_conv_b=v_ffn_conv_b, w_down=v_w_down)
    names = list(weights)
    xi, yi, ci = _position()
    chip = 2 * xi + yi
    chip_arr = jnp.reshape(chip, (1,)).astype(jnp.int32)
    core_arr = jnp.reshape(ci, (1,)).astype(jnp.int32)

    xs = x[0]
    target = loss_target[0]
    S, D = xs.shape

    big = ["w_in", "w_attn_out", "conv_pw_w", "w_out", "w_up", "w_down"]
    row_sharded = ("conv_pw_w", "w_out", "w_down")
    gathered = _allgather_call([weights[k][0].astype(BF16) for k in big], [conv_dw_w[0], ffn_conv_w[0]])
    W = {k: (g.reshape(-1, g.shape[2]) if k in row_sharded else g) for k, g in zip(big, gathered[:6])}
    dw_full = jnp.concatenate(list(gathered[6]), axis=1)
    fc_full = jnp.concatenate(list(gathered[7]), axis=1)
    P = dict(W, conv_dw_w=dw_full, ffn_conv_w=fc_full, b_gate=b_gate, rel_bias=rel_bias, conv_dw_b=conv_dw_b,
             conv_ln_g=conv_ln_g, conv_ln_b=conv_ln_b, norm_mix_pre=norm_mix_pre, norm_mix_post=norm_mix_post,
             norm_ffn_pre=norm_ffn_pre, norm_ffn_post=norm_ffn_post, ffn_conv_b=ffn_conv_b)
    loss_tile, grad_x, G = _local_step(xs, target, P)
    loss = lax.psum(loss_tile[0, 0], ("x", "y", "c"))

    recv1 = _sibling_exchange_call([G[k] for k in big])
    pair_f32, pair_bf16 = [], []
    for k, r1 in zip(big, recv1):
        s32, s16 = _pair_sum_call(G[k], r1, core_arr, f"pair_sum_{k}")
        pair_f32.append(s32)
        pair_bf16.append(s16)
    recv2 = _chip_exchange_call(pair_bf16)
    halves = [_chip_sum_call(p, r2, chip_arr, f"chip_sum_{k}") for k, p, r2 in zip(big, pair_f32, recv2)]
    reduced = dict(zip(big, _sibling_assemble_call(halves)))

    small = [k for k in names if k not in big]
    packed = _pack([G[k] for k in small])
    summed = _unpack(_small_allreduce_call(packed), [G[k].shape for k in small])
    for k, gsum in zip(small, summed):
        if k in ("conv_dw_w", "ffn_conv_w"):
            cols = weights[k].shape[2]
            reduced[k] = lax.dynamic_slice_in_dim(gsum, chip * cols, cols, axis=1)
        else:
            reduced[k] = gsum

    grads, deltas, new_m, new_v = {}, {}, {}, {}
    for k in big:
        d, mn, vn = _adamw_call(weights[k][0], reduced[k], m_in[k][0], v_in[k][0], f"adamw_{k}")
        grads[k], deltas[k], new_m[k], new_v[k] = reduced[k][None], d[None], mn[None], vn[None]
    flat2 = lambda t: t.reshape(-1, t.shape[-1]) if t.ndim == 3 else t
    pw = _pack([flat2(weights[k]) for k in small])
    pg = _pack([reduced[k] for k in small])
    pm = _pack([flat2(m_in[k]) for k in small])
    pv = _pack([flat2(v_in[k]) for k in small])
    pd, pmn, pvn = _adamw_call(pw, pg, pm, pv, "adamw_small")
    shapes = [weights[k].shape for k in small]
    for k, gk, dk_, mk, vk in zip(small, [reduced[k] for k in small], _unpack(pd, shapes), _unpack(pmn, shapes), _unpack(pvn, shapes)):
        grads[k], deltas[k], new_m[k], new_v[k] = gk.reshape(weights[k].shape), dk_, mk, vk

    return (loss, grad_x[None], *[grads[k] for k in names], *[deltas[k] for k in names],
            *[new_m[k] for k in names], *[new_v[k] for k in names])
```

```python
import functools
import math

import jax
import jax.numpy as jnp
import numpy as np
from jax import lax
from jax.experimental import pallas as pl
from jax.experimental.pallas import tpu as pltpu

F32 = jnp.float32
BF16 = jnp.bfloat16
MESH = pl.DeviceIdType.MESH

HEAD_DIM = 128
HEADS_PER_GROUP = 4
DILATED_PATTERNS = ((128, 1), (512, 4), (2048, 16))
N_GROUPS = 3
N_HEADS = N_GROUPS * HEADS_PER_GROUP
SPAN = 128
GROUP_WIDTH = HEADS_PER_GROUP * HEAD_DIM
CONV_WIDTH = 31
FFN_CONV_WIDTH = 3
N_BUCKETS = 32
MAX_DISTANCE = 2048
RMS_EPS = 1e-6
LN_EPS = 1e-5
NEG_INF = -1e30
ADAM_LR = 0.001
ADAM_B1 = 0.9
ADAM_B2 = 0.999
ADAM_EPS = 1e-08
ADAM_WD = 0.01
ADAM_STEP = 10

LANES = 128
SUBLANES = 8
ROW_TILE = 256
TIME_BLOCK = 128
CONV_PAD = 32
FFN_PAD = 8
VMEM_LIMIT = 56 << 20


def _params(sem=None, vmem=None):
    kw = {}
    if sem is not None:
        kw["dimension_semantics"] = sem
    if vmem is not None:
        kw["vmem_limit_bytes"] = vmem
    return pltpu.CompilerParams(**kw)


def _pick(n, cands):
    for c in cands:
        if n % c == 0:
            return c
    return n


ELEMENTWISE_TILE_BYTES = 1 << 20


def _row_tile(rows, cols):
    for c in (512, 256, 128, 64, 32, 16):
        if rows % c == 0 and c * cols * 4 <= ELEMENTWISE_TILE_BYTES:
            return c
    return 16 if rows % 16 == 0 else 8


N_CHIPS = 4
M_TILES = (1024, 1408, 512, 256, 128)
N_TILES = (512, 1408, 256, 128)
K_TILES = (2176, 2048, 1408, 1024, 512, 256, 128)


def _matmul(a, b, mode, name, out_shards=False, tm=None):
    assert a.dtype == BF16 and b.dtype == BF16, (name, a.dtype, b.dtype)
    b3 = b.ndim == 3
    tn = tk = None
    if mode == "nn":
        M, K = a.shape
        N = b.shape[-1] * (N_CHIPS if b3 else 1)
        tn = b.shape[-1] if b3 else None
    elif mode == "nt":
        M, K = a.shape
        N = b.shape[-2]
        tk = b.shape[-1] if b3 else None
    else:
        K, M = a.shape
        N = b.shape[1]
        tn = N // N_CHIPS if out_shards else None
    tm = tm or _pick(M, M_TILES)
    tn = tn or _pick(N, N_TILES)
    tk = tk or _pick(K, K_TILES)
    nk = K // tk
    dn = {"nn": (((1,), (0,)), ((), ())), "nt": (((1,), (1,)), ((), ())), "tn": (((0,), (0,)), ((), ()))}[mode]

    def body(a_ref, b_ref, o_ref):
        if nk == 1:
            o_ref[...] = lax.dot_general(a_ref[...], b_ref[...], dn, preferred_element_type=F32)
        else:
            @pl.when(pl.program_id(2) == 0)
            def _():
                o_ref[...] = jnp.zeros_like(o_ref)

            o_ref[...] += lax.dot_general(a_ref[...], b_ref[...], dn, preferred_element_type=F32)

    if mode == "tn":
        a_spec = pl.BlockSpec((tk, tm), lambda i, j, k: (k, i))
    else:
        a_spec = pl.BlockSpec((tm, tk), lambda i, j, k: (i, k))
    if mode == "nn":
        b_spec = pl.BlockSpec((None, tk, tn), lambda i, j, k: (j, k, 0)) if b3 else pl.BlockSpec((tk, tn), lambda i, j, k: (k, j))
    elif mode == "nt":
        b_spec = pl.BlockSpec((None, tn, tk), lambda i, j, k: (k, j, 0)) if b3 else pl.BlockSpec((tn, tk), lambda i, j, k: (j, k))
    else:
        b_spec = pl.BlockSpec((tk, tn), lambda i, j, k: (k, j))
    if out_shards:
        out_spec = pl.BlockSpec((None, tm, tn), lambda i, j, k: (j, i, 0))
        out_shape = jax.ShapeDtypeStruct((N_CHIPS, M, tn), F32)
    else:
        out_spec = pl.BlockSpec((tm, tn), lambda i, j, k: (i, j))
        out_shape = jax.ShapeDtypeStruct((M, N), F32)
    return pl.pallas_call(
        body, name=name, grid=(M // tm, N // tn, nk),
        in_specs=[a_spec, b_spec], out_specs=out_spec, out_shape=out_shape,
        compiler_params=_params(("parallel", "parallel", "arbitrary"), VMEM_LIMIT),
    )(a, b)


def _rms(x, g):
    r = lax.rsqrt(jnp.mean(x * x, axis=-1, keepdims=True) + RMS_EPS)
    return x * r * g


def _rms_bwd(x, g, dy):
    r = lax.rsqrt(jnp.mean(x * x, axis=-1, keepdims=True) + RMS_EPS)
    n = x * r
    dn = dy * g
    dx = r * (dn - n * jnp.mean(dn * n, axis=-1, keepdims=True))
    return dx, jnp.sum(dy * n, axis=0, keepdims=True)


def _sigmoid(x):
    return 1.0 / (1.0 + jnp.exp(-x))


_GELU_C = math.sqrt(2.0 / math.pi)


def _gelu(x):
    return 0.5 * x * (1.0 + jnp.tanh(_GELU_C * (x + 0.044715 * x * x * x)))


def _gelu_grad(x):
    t = jnp.tanh(_GELU_C * (x + 0.044715 * x * x * x))
    return 0.5 * (1.0 + t) + 0.5 * x * (1.0 - t * t) * _GELU_C * (1.0 + 3.0 * 0.044715 * x * x)


def _row_spec(width, col_block=0):
    return pl.BlockSpec((ROW_TILE, width), lambda i: (i, col_block))


def _vec_spec(width, col_block=0):
    return pl.BlockSpec((1, width), lambda i: (0, col_block))


def _accumulate(ref, part):
    @pl.when(pl.program_id(0) == 0)
    def _():
        ref[...] = part

    @pl.when(pl.program_id(0) > 0)
    def _():
        ref[...] += part


def _rms_fwd_call(x, g):
    S, D = x.shape

    def body(x_ref, g_ref, h_ref):
        h_ref[...] = _rms(x_ref[...], g_ref[...]).astype(BF16)

    return pl.pallas_call(
        body, name="rms_mix_pre", grid=(S // ROW_TILE,),
        in_specs=[_row_spec(D), _vec_spec(D)], out_specs=_row_spec(D),
        out_shape=jax.ShapeDtypeStruct((S, D), BF16),
        compiler_params=_params(("parallel",)),
    )(x, g)


def _ln_silu_call(c1, g, b):
    S, C = c1.shape

    def body(c_ref, g_ref, b_ref, o_ref):
        xv = c_ref[...]
        mu = jnp.mean(xv, axis=-1, keepdims=True)
        xc = xv - mu
        var = jnp.mean(xc * xc, axis=-1, keepdims=True)
        z = xc * lax.rsqrt(var + LN_EPS) * g_ref[...] + b_ref[...]
        o_ref[...] = (z * _sigmoid(z)).astype(BF16)

    return pl.pallas_call(
        body, name="conv_ln_silu", grid=(S // ROW_TILE,),
        in_specs=[_row_spec(C), _vec_spec(C), _vec_spec(C)], out_specs=_row_spec(C),
        out_shape=jax.ShapeDtypeStruct((S, C), BF16),
        compiler_params=_params(("parallel",)),
    )(c1, g, b)


def _ln_silu_bwd_call(c1, g, b, dc):
    S, C = c1.shape

    def body(c_ref, g_ref, b_ref, dc_ref, dx_ref, dg_ref, db_ref):
        xv = c_ref[...]
        mu = jnp.mean(xv, axis=-1, keepdims=True)
        xc = xv - mu
        rs = lax.rsqrt(jnp.mean(xc * xc, axis=-1, keepdims=True) + LN_EPS)
        xh = xc * rs
        z = xh * g_ref[...] + b_ref[...]
        sg = _sigmoid(z)
        dz = dc_ref[...] * (sg * (1.0 + z * (1.0 - sg)))
        dxh = dz * g_ref[...]
        dx_ref[...] = rs * (dxh - jnp.mean(dxh, axis=-1, keepdims=True) - xh * jnp.mean(dxh * xh, axis=-1, keepdims=True))
        _accumulate(dg_ref, jnp.sum(dz * xh, axis=0, keepdims=True))
        _accumulate(db_ref, jnp.sum(dz, axis=0, keepdims=True))

    return pl.pallas_call(
        body, name="conv_ln_silu_bwd", grid=(S // ROW_TILE,),
        in_specs=[_row_spec(C), _vec_spec(C), _vec_spec(C), _row_spec(C)],
        out_specs=[_row_spec(C), _vec_spec(C), _vec_spec(C)],
        out_shape=[jax.ShapeDtypeStruct((S, C), F32), jax.ShapeDtypeStruct((1, C), F32), jax.ShapeDtypeStruct((1, C), F32)],
        compiler_params=_params(("arbitrary",)),
    )(c1, g, b, dc)


def _mix_call(proj, gate_col0, b_gate, y_a, y_c):
    S, D = y_a.shape
    w = 512
    nc = D // w
    ga0, gc0 = gate_col0 // w, (gate_col0 + D) // w

    def body(ga_ref, gc_ref, ba_ref, bc_ref, ya_ref, yc_ref, o_ref):
        o_ref[...] = (_sigmoid(ga_ref[...] + ba_ref[...]) * ya_ref[...]
                      + _sigmoid(gc_ref[...] + bc_ref[...]) * yc_ref[...]).astype(BF16)

    tile = lambda off: pl.BlockSpec((ROW_TILE, w), lambda i, j: (i, off + j))
    vec = lambda off: pl.BlockSpec((1, w), lambda i, j: (0, off + j))
    return pl.pallas_call(
        body, name="gate_mix", grid=(S // ROW_TILE, nc),
        in_specs=[tile(ga0), tile(gc0), vec(0), vec(nc), tile(0), tile(0)],
        out_specs=tile(0), out_shape=jax.ShapeDtypeStruct((S, D), BF16),
        compiler_params=_params(("parallel", "parallel")),
    )(proj, proj, b_gate, b_gate, y_a, y_c)


def _mix_bwd_call(dmixed, proj, gate_col0, b_gate, y_a, y_c):
    S, D = y_a.shape
    w = 512
    nc = D // w
    ga0, gc0 = gate_col0 // w, (gate_col0 + D) // w

    def body(dm_ref, ga_ref, gc_ref, ba_ref, bc_ref, ya_ref, yc_ref, dya_ref, dyc_ref, dga_ref, dgc_ref, dba_ref, dbc_ref):
        dm = dm_ref[...]
        sa = _sigmoid(ga_ref[...] + ba_ref[...])
        sc = _sigmoid(gc_ref[...] + bc_ref[...])
        dya_ref[...] = (dm * sa).astype(BF16)
        dyc_ref[...] = (dm * sc).astype(BF16)
        dga = dm * ya_ref[...] * sa * (1.0 - sa)
        dgc = dm * yc_ref[...] * sc * (1.0 - sc)
        dga_ref[...] = dga.astype(BF16)
        dgc_ref[...] = dgc.astype(BF16)
        pa = jnp.sum(dga, axis=0, keepdims=True)
        pc = jnp.sum(dgc, axis=0, keepdims=True)

        @pl.when(pl.program_id(1) == 0)
        def _():
            dba_ref[...] = pa
            dbc_ref[...] = pc

        @pl.when(pl.program_id(1) > 0)
        def _():
            dba_ref[...] += pa
            dbc_ref[...] += pc

    tile = lambda off: pl.BlockSpec((ROW_TILE, w), lambda j, i: (i, off + j))
    vec = lambda off: pl.BlockSpec((1, w), lambda j, i: (0, off + j))
    return pl.pallas_call(
        body, name="gate_mix_bwd", grid=(nc, S // ROW_TILE),
        in_specs=[tile(0), tile(ga0), tile(gc0), vec(0), vec(nc), tile(0), tile(0)],
        out_specs=[tile(0), tile(0), tile(0), tile(0), vec(0), vec(0)],
        out_shape=[jax.ShapeDtypeStruct((S, D), BF16)] * 4 + [
                   jax.ShapeDtypeStruct((1, D), F32), jax.ShapeDtypeStruct((1, D), F32)],
        compiler_params=_params(("parallel", "arbitrary")),
    )(dmixed, proj, proj, b_gate, b_gate, y_a, y_c)


def _res1_call(x, out, g_post, g_pre):
    S, D = x.shape

    def body(x_ref, o_ref, gp_ref, gq_ref, x1_ref, h2_ref):
        x1 = x_ref[...] + _rms(o_ref[...], gp_ref[...])
        x1_ref[...] = x1
        h2_ref[...] = _rms(x1, gq_ref[...]).astype(BF16)

    return pl.pallas_call(
        body, name="residual_mix", grid=(S // ROW_TILE,),
        in_specs=[_row_spec(D), _row_spec(D), _vec_spec(D), _vec_spec(D)],
        out_specs=[_row_spec(D), _row_spec(D)],
        out_shape=[jax.ShapeDtypeStruct((S, D), F32), jax.ShapeDtypeStruct((S, D), BF16)],
        compiler_params=_params(("parallel",)),
    )(x, out, g_post, g_pre)


def _loss_call(y, x1, g_post, target):
    S, D = y.shape

    def body(y_ref, x1_ref, g_ref, t_ref, loss_ref, dx_ref):
        err = x1_ref[...] + _rms(y_ref[...], g_ref[...]) - t_ref[...]
        dx_ref[...] = err * (1.0 / D)
        part = 0.5 * jnp.sum(jnp.mean(err * err, axis=-1, keepdims=True), axis=0, keepdims=True)
        _accumulate(loss_ref, jnp.broadcast_to(part, (SUBLANES, LANES)))

    return pl.pallas_call(
        body, name="residual_ffn_loss", grid=(S // ROW_TILE,),
        in_specs=[_row_spec(D), _row_spec(D), _vec_spec(D), _row_spec(D)],
        out_specs=[pl.BlockSpec((SUBLANES, LANES), lambda i: (0, 0)), _row_spec(D)],
        out_shape=[jax.ShapeDtypeStruct((SUBLANES, LANES), F32), jax.ShapeDtypeStruct((S, D), F32)],
        compiler_params=_params(("arbitrary",)),
    )(y, x1, g_post, target)


def _rms_bwd_call(x, g, dy, name):
    S, D = x.shape

    def body(x_ref, g_ref, dy_ref, dx_ref, dg_ref):
        dx, dg = _rms_bwd(x_ref[...], g_ref[...], dy_ref[...])
        dx_ref[...] = dx.astype(BF16)
        _accumulate(dg_ref, dg)

    return pl.pallas_call(
        body, name=name, grid=(S // ROW_TILE,),
        in_specs=[_row_spec(D), _vec_spec(D), _row_spec(D)],
        out_specs=[_row_spec(D), _vec_spec(D)],
        out_shape=[jax.ShapeDtypeStruct((S, D), BF16), jax.ShapeDtypeStruct((1, D), F32)],
        compiler_params=_params(("arbitrary",)),
    )(x, g, dy)


def _mid_bwd_call(x1, g_pre, dh2, dx2, out, g_post):
    S, D = x1.shape

    def body(x1_ref, gq_ref, dh_ref, dx2_ref, o_ref, gp_ref, dx1_ref, do_ref, dgq_ref, dgp_ref):
        d, dgq = _rms_bwd(x1_ref[...], gq_ref[...], dh_ref[...])
        dx1 = dx2_ref[...] + d
        dx1_ref[...] = dx1
        do, dgp = _rms_bwd(o_ref[...], gp_ref[...], dx1)
        do_ref[...] = do.astype(BF16)
        _accumulate(dgq_ref, dgq)
        _accumulate(dgp_ref, dgp)

    return pl.pallas_call(
        body, name="residual_mix_bwd", grid=(S // ROW_TILE,),
        in_specs=[_row_spec(D), _vec_spec(D), _row_spec(D), _row_spec(D), _row_spec(D), _vec_spec(D)],
        out_specs=[_row_spec(D), _row_spec(D), _vec_spec(D), _vec_spec(D)],
        out_shape=[jax.ShapeDtypeStruct((S, D), F32), jax.ShapeDtypeStruct((S, D), BF16)] + [jax.ShapeDtypeStruct((1, D), F32)] * 2,
        compiler_params=_params(("arbitrary",)),
    )(x1, g_pre, dh2, dx2, out, g_post)


def _in_bwd_call(x, g, dh1, dx1):
    S, D = x.shape

    def body(x_ref, g_ref, dh_ref, dx1_ref, gx_ref, dg_ref):
        d, dg = _rms_bwd(x_ref[...], g_ref[...], dh_ref[...])
        gx_ref[...] = dx1_ref[...] + d
        _accumulate(dg_ref, dg)

    return pl.pallas_call(
        body, name="rms_mix_pre_bwd", grid=(S // ROW_TILE,),
        in_specs=[_row_spec(D), _vec_spec(D), _row_spec(D), _row_spec(D)],
        out_specs=[_row_spec(D), _vec_spec(D)],
        out_shape=[jax.ShapeDtypeStruct((S, D), F32), jax.ShapeDtypeStruct((1, D), F32)],
        compiler_params=_params(("arbitrary",)),
    )(x, g, dh1, dx1)


def _bucket_table(dilation):
    qi = np.arange(SPAN)[:, None]
    ki = np.arange(2 * SPAN)[None, :]
    dist = np.maximum(qi + SPAN - ki, 0) * dilation
    max_exact = N_BUCKETS // 2
    d = np.maximum(dist, 1).astype(np.float64)
    large = max_exact + (np.log(d / max_exact) / math.log(MAX_DISTANCE / max_exact) * (N_BUCKETS - max_exact)).astype(np.int32)
    large = np.minimum(large, N_BUCKETS - 1)
    return np.where(dist < max_exact, dist, large).astype(np.int32)


def _bucket_tables():
    return jnp.asarray(np.stack([_bucket_table(r) for _, r in DILATED_PATTERNS]))


def _bias_table_call(rel_bias, buckets):
    def body(rb_ref, bk_ref, o_ref):
        for h in range(N_HEADS):
            bk = bk_ref[h // HEADS_PER_GROUP]

            def step(b, acc):
                return jnp.where(bk == b, rb_ref[b, h], acc)

            o_ref[h] = lax.fori_loop(0, N_BUCKETS, step, jnp.zeros((SPAN, 2 * SPAN), F32))

    return pl.pallas_call(
        body, name="rel_bias_table",
        in_specs=[pl.BlockSpec(memory_space=pltpu.SMEM), pl.BlockSpec(memory_space=pltpu.VMEM)],
        out_specs=pl.BlockSpec(memory_space=pltpu.VMEM),
        out_shape=jax.ShapeDtypeStruct((N_HEADS, SPAN, 2 * SPAN), F32),
    )(rel_bias, buckets)


def _bias_grad_call(dbias, buckets):
    def body(db_ref, bk_ref, o_ref, rows_ref):
        for h in range(N_HEADS):
            bk = bk_ref[h // HEADS_PER_GROUP]
            dv = db_ref[h]

            def step(b, carry):
                rows_ref[h, b] = jnp.sum(jnp.where(bk == b, dv, 0.0), axis=0, keepdims=True)
                return carry

            lax.fori_loop(0, N_BUCKETS, step, 0)
        o_ref[...] = jnp.sum(rows_ref[...], axis=-1, keepdims=True)

    out = pl.pallas_call(
        body, name="rel_bias_grad",
        in_specs=[pl.BlockSpec(memory_space=pltpu.VMEM), pl.BlockSpec(memory_space=pltpu.VMEM)],
        out_specs=pl.BlockSpec(memory_space=pltpu.VMEM),
        out_shape=jax.ShapeDtypeStruct((N_HEADS, N_BUCKETS, 1, 1), F32),
        scratch_shapes=[pltpu.VMEM((N_HEADS, N_BUCKETS, 1, 2 * SPAN), F32)],
    )(dbias, buckets)
    return out.reshape(N_HEADS, N_BUCKETS).T


def _dot_nt(a, b):
    return lax.dot_general(a, b, (((1,), (1,)), ((), ())), preferred_element_type=F32)


def _dot_nn(a, b):
    return lax.dot_general(a, b, (((1,), (0,)), ((), ())), preferred_element_type=F32)


def _dot_tn(a, b):
    return lax.dot_general(a, b, (((0,), (0,)), ((), ())), preferred_element_type=F32)


def _band_masks(n, nb):
    qi = lax.broadcasted_iota(jnp.int32, (SPAN, SPAN), 0)
    ki = lax.broadcasted_iota(jnp.int32, (SPAN, SPAN), 1)
    prev_ok = jnp.logical_and(ki >= qi, n > 0)
    cur_ok = ki <= qi
    next_ok = jnp.logical_and(ki >= qi, n < nb - 1)
    return prev_ok, cur_ok, next_ok


def _regroup_call(src, col_block, r, inverse, name):
    S = src.shape[0]
    L = S // r
    nt = GROUP_WIDTH // LANES

    def body(x_ref, o_ref):
        for rho in range(r):
            if inverse:
                o_ref[pl.ds(rho, L, stride=r), :] = x_ref[rho * L:(rho + 1) * L, :]
            else:
                o_ref[rho * L:(rho + 1) * L, :] = x_ref[pl.ds(rho, L, stride=r), :]

    return pl.pallas_call(
        body, name=name, grid=(nt,),
        in_specs=[pl.BlockSpec((S, LANES), lambda i: (0, col_block * nt + i))],
        out_specs=pl.BlockSpec((S, LANES), lambda i: (0, i)),
        out_shape=jax.ShapeDtypeStruct((S, GROUP_WIDTH), F32),
        compiler_params=_params(("parallel",)),
    )(src)


def _to_group_order(src, col_block, group, name):
    r = DILATED_PATTERNS[group][1]
    if r == 1:
        return src, col_block
    return _regroup_call(src, col_block, r, False, name), 0


def _to_token_order(arr, group, name):
    r = DILATED_PATTERNS[group][1]
    return arr if r == 1 else _regroup_call(arr, 0, r, True, name)


def _attn_fwd_call(q, k, v, bias, group):
    S = q[0].shape[0]
    r = DILATED_PATTERNS[group][1]
    nb = S // r // SPAN
    scale = HEAD_DIM ** -0.5

    def body(q_ref, kp_ref, kc_ref, vp_ref, vc_ref, b_ref, o_ref, lse_ref):
        n = pl.program_id(1)
        prev_ok, cur_ok, _ = _band_masks(n, nb)
        for j in range(HEADS_PER_GROUP):
            sl = slice(j * HEAD_DIM, (j + 1) * HEAD_DIM)
            q = q_ref[:, sl].astype(BF16)
            sp = _dot_nt(q, kp_ref[:, sl].astype(BF16)) * scale + b_ref[j, :, :SPAN]
            sc = _dot_nt(q, kc_ref[:, sl].astype(BF16)) * scale + b_ref[j, :, SPAN:]
            sp = jnp.where(prev_ok, sp, NEG_INF)
            sc = jnp.where(cur_ok, sc, NEG_INF)
            m = jnp.maximum(jnp.max(sp, axis=-1, keepdims=True), jnp.max(sc, axis=-1, keepdims=True))
            pp = jnp.exp(sp - m)
            pc = jnp.exp(sc - m)
            den = jnp.sum(pp, axis=-1, keepdims=True) + jnp.sum(pc, axis=-1, keepdims=True)
            o_ref[:, sl] = (_dot_nn(pp.astype(BF16), vp_ref[:, sl].astype(BF16))
                            + _dot_nn(pc.astype(BF16), vc_ref[:, sl].astype(BF16))) / den
            lse_ref[:, sl] = jnp.broadcast_to(m + jnp.log(den), (SPAN, HEAD_DIM))

    blk = (SPAN, GROUP_WIDTH)
    cur = lambda cb: pl.BlockSpec(blk, lambda rho, n: (rho * nb + n, cb))
    prev = lambda cb: pl.BlockSpec(blk, lambda rho, n: (rho * nb + jnp.maximum(n - 1, 0), cb))
    return pl.pallas_call(
        body, name=f"attn_fwd_g{group}", grid=(r, nb),
        in_specs=[cur(q[1]), prev(k[1]), cur(k[1]), prev(v[1]), cur(v[1]),
                  pl.BlockSpec((HEADS_PER_GROUP, SPAN, 2 * SPAN), lambda rho, n: (group, 0, 0))],
        out_specs=[cur(0)] * 2,
        out_shape=[jax.ShapeDtypeStruct((S, GROUP_WIDTH), F32)] * 2,
        compiler_params=_params(("parallel", "parallel")),
    )(q[0], k[0], k[0], v[0], v[0], bias)


def _attn_merge_call(parts):
    S = parts[0].shape[0]

    def body(o1, s1, o2, s2, o3, s3, a_ref, ab_ref, lse_ref):
        mx = jnp.maximum(jnp.maximum(s1[...], s2[...]), s3[...])
        w1 = jnp.exp(s1[...] - mx)
        w2 = jnp.exp(s2[...] - mx)
        w3 = jnp.exp(s3[...] - mx)
        den = w1 + w2 + w3
        a = (w1 * o1[...] + w2 * o2[...] + w3 * o3[...]) / den
        a_ref[...] = a
        ab_ref[...] = a.astype(BF16)
        lse_ref[...] = mx + jnp.log(den)

    return pl.pallas_call(
        body, name="attn_merge", grid=(S // ROW_TILE,),
        in_specs=[_row_spec(GROUP_WIDTH)] * 6, out_specs=[_row_spec(GROUP_WIDTH)] * 3,
        out_shape=[jax.ShapeDtypeStruct((S, GROUP_WIDTH), F32), jax.ShapeDtypeStruct((S, GROUP_WIDTH), BF16),
                   jax.ShapeDtypeStruct((S, GROUP_WIDTH), F32)],
        compiler_params=_params(("parallel",)),
    )(*parts)


def _attn_delta_call(a, da):
    S = a.shape[0]

    def body(a_ref, da_ref, d_ref):
        for j in range(HEADS_PER_GROUP):
            sl = slice(j * HEAD_DIM, (j + 1) * HEAD_DIM)
            d = jnp.sum(a_ref[:, sl] * da_ref[:, sl], axis=-1, keepdims=True)
            d_ref[:, sl] = jnp.broadcast_to(d, (ROW_TILE, HEAD_DIM))

    return pl.pallas_call(
        body, name="attn_delta", grid=(S // ROW_TILE,),
        in_specs=[_row_spec(GROUP_WIDTH)] * 2, out_specs=_row_spec(GROUP_WIDTH),
        out_shape=jax.ShapeDtypeStruct((S, GROUP_WIDTH), F32),
        compiler_params=_params(("parallel",)),
    )(a, da)


def _attn_bwd_call(q, k, v, bias, da, lse, delta, group):
    S = q[0].shape[0]
    r = DILATED_PATTERNS[group][1]
    nb = S // r // SPAN
    scale = HEAD_DIM ** -0.5

    def body(q_ref, qn_ref, kp_ref, kc_ref, vp_ref, vc_ref, b_ref, da_ref, dan_ref, lse_ref, lsen_ref, dl_ref, dln_ref,
             dq_ref, dk_ref, dv_ref, db_ref):
        n = pl.program_id(1)
        prev_ok, cur_ok, next_ok = _band_masks(n, nb)
        first = jnp.logical_and(pl.program_id(0) == 0, n == 0)
        for j in range(HEADS_PER_GROUP):
            sl = slice(j * HEAD_DIM, (j + 1) * HEAD_DIM)
            q = q_ref[:, sl].astype(BF16)
            qn = qn_ref[:, sl].astype(BF16)
            kp = kp_ref[:, sl].astype(BF16)
            kc = kc_ref[:, sl].astype(BF16)
            vp = vp_ref[:, sl].astype(BF16)
            vc = vc_ref[:, sl].astype(BF16)
            dav = da_ref[:, sl].astype(BF16)
            dan = dan_ref[:, sl].astype(BF16)
            bp = b_ref[j, :, :SPAN]
            bc = b_ref[j, :, SPAN:]
            pp = jnp.exp(jnp.where(prev_ok, _dot_nt(q, kp) * scale + bp, NEG_INF) - lse_ref[:, sl])
            pc = jnp.exp(jnp.where(cur_ok, _dot_nt(q, kc) * scale + bc, NEG_INF) - lse_ref[:, sl])
            pn = jnp.exp(jnp.where(next_ok, _dot_nt(qn, kc) * scale + bp, NEG_INF) - lsen_ref[:, sl])
            dsp = pp * (_dot_nt(dav, vp) - dl_ref[:, sl])
            dsc = pc * (_dot_nt(dav, vc) - dl_ref[:, sl])
            dsn = pn * (_dot_nt(dan, vc) - dln_ref[:, sl])
            dsp_b, dsc_b, dsn_b = dsp.astype(BF16), dsc.astype(BF16), dsn.astype(BF16)
            dq_ref[:, sl] = (_dot_nn(dsp_b, kp) + _dot_nn(dsc_b, kc)) * scale
            dk_ref[:, sl] = (_dot_tn(dsc_b, q) + _dot_tn(dsn_b, qn)) * scale
            dv_ref[:, sl] = _dot_tn(pc.astype(BF16), dav) + _dot_tn(pn.astype(BF16), dan)

            @pl.when(first)
            def _():
                db_ref[j, :, :SPAN] = dsp
                db_ref[j, :, SPAN:] = dsc

            @pl.when(jnp.logical_not(first))
            def _():
                db_ref[j, :, :SPAN] += dsp
                db_ref[j, :, SPAN:] += dsc

    blk = (SPAN, GROUP_WIDTH)
    cur = lambda cb: pl.BlockSpec(blk, lambda rho, n: (rho * nb + n, cb))
    prev = lambda cb: pl.BlockSpec(blk, lambda rho, n: (rho * nb + jnp.maximum(n - 1, 0), cb))
    nxt = lambda cb: pl.BlockSpec(blk, lambda rho, n: (rho * nb + jnp.minimum(n + 1, nb - 1), cb))
    band = (HEADS_PER_GROUP, SPAN, 2 * SPAN)
    return pl.pallas_call(
        body, name=f"attn_bwd_g{group}", grid=(r, nb),
        in_specs=[cur(q[1]), nxt(q[1]), prev(k[1]), cur(k[1]), prev(v[1]), cur(v[1]),
                  pl.BlockSpec(band, lambda rho, n: (group, 0, 0)),
                  cur(0), nxt(0), cur(0), nxt(0), cur(0), nxt(0)],
        out_specs=[cur(0), cur(0), cur(0), pl.BlockSpec(band, lambda rho, n: (0, 0, 0))],
        out_shape=[jax.ShapeDtypeStruct((S, GROUP_WIDTH), F32)] * 3 + [jax.ShapeDtypeStruct(band, F32)],
        compiler_params=_params(("arbitrary", "arbitrary")),
    )(q[0], q[0], k[0], k[0], v[0], v[0], bias, da, da, lse, lse, delta, delta)


def _taps(width):
    return [(k, (width - 1 - k) // SUBLANES, (width - 1 - k) % SUBLANES) for k in range(width)]


def _shifted(win, width, pad, up):
    total = win.shape[0]
    for b in range(SUBLANES):
        taps = [(k, a) for k, a, bb in _taps(width) if bb == b]
        if not taps:
            continue
        if up:
            rolled = win if b == 0 else pltpu.roll(win, total - b, axis=0)
        else:
            rolled = win if b == 0 else pltpu.roll(win, b, axis=0)
        for k, a in taps:
            start = SUBLANES * a if up else pad - SUBLANES * a
            yield k, rolled[start:start + TIME_BLOCK, :]


def _conv_block(win, w_ref, width, pad):
    acc = None
    for k, rows in _shifted(win, width, pad, up=False):
        term = w_ref[k:k + 1, :] * rows
        acc = term if acc is None else acc + term
    return acc


def _conv_transpose_block(win, w_ref, width, pad):
    acc = None
    for k, rows in _shifted(win, width, pad, up=True):
        term = w_ref[k:k + 1, :] * rows
        acc = term if acc is None else acc + term
    return acc


def _conv_weight_grad(win, dy, dw_ref, width, pad):
    for k, rows in _shifted(win, width, pad, up=False):
        dw_ref[k:k + 1, :] += jnp.sum(dy * rows, axis=0, keepdims=True)


def _time_loop(S, step):
    def it(tb, carry):
        step(pl.multiple_of(tb * TIME_BLOCK, TIME_BLOCK))
        return carry

    lax.fori_loop(0, S // TIME_BLOCK, it, 0)


def _conv_fwd_call(proj, col0, w, b):
    S = proj.shape[0]
    C = w.shape[1]
    nt = C // LANES
    v0, g0 = col0 // LANES, (col0 + C) // LANES

    def body(val_ref, gate_ref, w_ref, b_ref, o_ref, pad_ref):
        pad_ref[0:CONV_PAD, :] = jnp.zeros((CONV_PAD, LANES), F32)
        pad_ref[CONV_PAD:, :] = val_ref[...] * _sigmoid(gate_ref[...])

        def step(t0):
            win = pad_ref[pl.ds(t0, TIME_BLOCK + CONV_PAD), :]
            o_ref[pl.ds(t0, TIME_BLOCK), :] = _conv_block(win, w_ref, CONV_WIDTH, CONV_PAD) + b_ref[...]

        _time_loop(S, step)

    seq = lambda off: pl.BlockSpec((S, LANES), lambda i: (0, off + i))
    return pl.pallas_call(
        body, name="conv_module", grid=(nt,),
        in_specs=[seq(v0), seq(g0), pl.BlockSpec((CONV_WIDTH, LANES), lambda i: (0, i)), pl.BlockSpec((1, LANES), lambda i: (0, i))],
        out_specs=seq(0), out_shape=jax.ShapeDtypeStruct((S, C), F32),
        scratch_shapes=[pltpu.VMEM((S + CONV_PAD, LANES), F32)],
        compiler_params=_params(("parallel",)),
    )(proj, proj, w, b)


def _conv_bwd_call(proj, col0, w, dc1):
    S = proj.shape[0]
    C = w.shape[1]
    nt = C // LANES
    v0, g0 = col0 // LANES, (col0 + C) // LANES

    def body(val_ref, gate_ref, w_ref, dy_ref, dval_ref, dgate_ref, dw_ref, db_ref, xpad_ref, dpad_ref, dwacc_ref):
        xpad_ref[0:CONV_PAD, :] = jnp.zeros((CONV_PAD, LANES), F32)
        xpad_ref[CONV_PAD:, :] = val_ref[...] * _sigmoid(gate_ref[...])
        dpad_ref[0:S, :] = dy_ref[...]
        dpad_ref[S:, :] = jnp.zeros((CONV_PAD, LANES), F32)
        dwacc_ref[...] = jnp.zeros_like(dwacc_ref)

        def step(t0):
            rows = pl.ds(t0, TIME_BLOCK)
            _conv_weight_grad(xpad_ref[pl.ds(t0, TIME_BLOCK + CONV_PAD), :], dy_ref[rows, :], dwacc_ref, CONV_WIDTH, CONV_PAD)
            dc0 = _conv_transpose_block(dpad_ref[pl.ds(t0, TIME_BLOCK + CONV_PAD), :], w_ref, CONV_WIDTH, CONV_PAD)
            sg = _sigmoid(gate_ref[rows, :])
            dval_ref[rows, :] = (dc0 * sg).astype(BF16)
            dgate_ref[rows, :] = (dc0 * val_ref[rows, :] * sg * (1.0 - sg)).astype(BF16)

        _time_loop(S, step)
        dw_ref[...] = dwacc_ref[...]
        db_ref[...] = jnp.sum(dy_ref[...], axis=0, keepdims=True)

    seq = lambda off: pl.BlockSpec((S, LANES), lambda i: (0, off + i))
    return pl.pallas_call(
        body, name="conv_module_bwd", grid=(nt,),
        in_specs=[seq(v0), seq(g0), pl.BlockSpec((CONV_WIDTH, LANES), lambda i: (0, i)), seq(0)],
        out_specs=[seq(0), seq(0), pl.BlockSpec((CONV_PAD, LANES), lambda i: (0, i)), pl.BlockSpec((1, LANES), lambda i: (0, i))],
        out_shape=[jax.ShapeDtypeStruct((S, C), BF16), jax.ShapeDtypeStruct((S, C), BF16),
                   jax.ShapeDtypeStruct((CONV_PAD, C), F32), jax.ShapeDtypeStruct((1, C), F32)],
        scratch_shapes=[pltpu.VMEM((S + CONV_PAD, LANES), F32), pltpu.VMEM((S + CONV_PAD, LANES), F32),
                        pltpu.VMEM((CONV_PAD, LANES), F32)],
        compiler_params=_params(("parallel",)),
    )(proj, proj, w, dc1)


def _ffn_fwd_call(u, w, b):
    S, C2 = u.shape
    C = C2 // 2
    nt = C // LANES

    def body(ug_ref, uv_ref, wg_ref, wv_ref, bg_ref, bv_ref, f_ref, pg_ref, pv_ref):
        zeros = jnp.zeros((FFN_PAD, LANES), F32)
        pg_ref[0:FFN_PAD, :] = zeros
        pv_ref[0:FFN_PAD, :] = zeros
        pg_ref[FFN_PAD:, :] = ug_ref[...]
        pv_ref[FFN_PAD:, :] = uv_ref[...]

        def step(t0):
            win = pl.ds(t0, TIME_BLOCK + FFN_PAD)
            cg = _conv_block(pg_ref[win, :], wg_ref, FFN_CONV_WIDTH, FFN_PAD) + bg_ref[...]
            cv = _conv_block(pv_ref[win, :], wv_ref, FFN_CONV_WIDTH, FFN_PAD) + bv_ref[...]
            f_ref[pl.ds(t0, TIME_BLOCK), :] = (_gelu(cg) * cv).astype(BF16)

        _time_loop(S, step)

    seq = lambda off: pl.BlockSpec((S, LANES), lambda i: (0, off + i))
    wsp = lambda off: pl.BlockSpec((FFN_CONV_WIDTH, LANES), lambda i: (0, off + i))
    bsp = lambda off: pl.BlockSpec((1, LANES), lambda i: (0, off + i))
    return pl.pallas_call(
        body, name="ffn_conv_geglu", grid=(nt,),
        in_specs=[seq(0), seq(nt), wsp(0), wsp(nt), bsp(0), bsp(nt)],
        out_specs=seq(0), out_shape=jax.ShapeDtypeStruct((S, C), BF16),
        scratch_shapes=[pltpu.VMEM((S + FFN_PAD, LANES), F32)] * 2,
        compiler_params=_params(("parallel",)),
    )(u, u, w, w, b, b)


def _ffn_bwd_call(u, w, b, df):
    S, C2 = u.shape
    C = C2 // 2
    nt = C // LANES

    def body(ug_ref, uv_ref, wg_ref, wv_ref, bg_ref, bv_ref, df_ref,
             dug_ref, duv_ref, dwg_ref, dwv_ref, dbg_ref, dbv_ref,
             pg_ref, pv_ref, dg_ref, dv_ref, dwg_acc, dwv_acc, dbg_acc, dbv_acc):
        zeros = jnp.zeros((FFN_PAD, LANES), F32)
        pg_ref[0:FFN_PAD, :] = zeros
        pv_ref[0:FFN_PAD, :] = zeros
        pg_ref[FFN_PAD:, :] = ug_ref[...]
        pv_ref[FFN_PAD:, :] = uv_ref[...]
        dg_ref[S:, :] = zeros
        dv_ref[S:, :] = zeros
        dwg_acc[...] = jnp.zeros_like(dwg_acc)
        dwv_acc[...] = jnp.zeros_like(dwv_acc)
        dbg_acc[...] = jnp.zeros_like(dbg_acc)
        dbv_acc[...] = jnp.zeros_like(dbv_acc)

        def first(t0):
            win = pl.ds(t0, TIME_BLOCK + FFN_PAD)
            rows = pl.ds(t0, TIME_BLOCK)
            xg = pg_ref[win, :]
            xv = pv_ref[win, :]
            cg = _conv_block(xg, wg_ref, FFN_CONV_WIDTH, FFN_PAD) + bg_ref[...]
            cv = _conv_block(xv, wv_ref, FFN_CONV_WIDTH, FFN_PAD) + bv_ref[...]
            dfb = df_ref[rows, :]
            dcg = dfb * cv * _gelu_grad(cg)
            dcv = dfb * _gelu(cg)
            dg_ref[rows, :] = dcg
            dv_ref[rows, :] = dcv
            _conv_weight_grad(xg, dcg, dwg_acc, FFN_CONV_WIDTH, FFN_PAD)
            _conv_weight_grad(xv, dcv, dwv_acc, FFN_CONV_WIDTH, FFN_PAD)
            dbg_acc[...] += jnp.sum(dcg, axis=0, keepdims=True)
            dbv_acc[...] += jnp.sum(dcv, axis=0, keepdims=True)

        def second(t0):
            win = pl.ds(t0, TIME_BLOCK + FFN_PAD)
            rows = pl.ds(t0, TIME_BLOCK)
            dug_ref[rows, :] = _conv_transpose_block(dg_ref[win, :], wg_ref, FFN_CONV_WIDTH, FFN_PAD).astype(BF16)
            duv_ref[rows, :] = _conv_transpose_block(dv_ref[win, :], wv_ref, FFN_CONV_WIDTH, FFN_PAD).astype(BF16)

        _time_loop(S, first)
        _time_loop(S, second)
        dwg_ref[...] = dwg_acc[...]
        dwv_ref[...] = dwv_acc[...]
        dbg_ref[...] = dbg_acc[...]
        dbv_ref[...] = dbv_acc[...]

    seq = lambda off: pl.BlockSpec((S, LANES), lambda i: (0, off + i))
    wsp = lambda off: pl.BlockSpec((FFN_CONV_WIDTH, LANES), lambda i: (0, off + i))
    bsp = lambda off: pl.BlockSpec((1, LANES), lambda i: (0, off + i))
    return pl.pallas_call(
        body, name="ffn_conv_geglu_bwd", grid=(nt,),
        in_specs=[seq(0), seq(nt), wsp(0), wsp(nt), bsp(0), bsp(nt), seq(0)],
        out_specs=[seq(0), seq(0), pl.BlockSpec((SUBLANES, LANES), lambda i: (0, i)), pl.BlockSpec((SUBLANES, LANES), lambda i: (0, i)),
                   bsp(0), bsp(0)],
        out_shape=[jax.ShapeDtypeStruct((S, C), BF16)] * 2 + [jax.ShapeDtypeStruct((SUBLANES, C), F32)] * 2
        + [jax.ShapeDtypeStruct((1, C), F32)] * 2,
        scratch_shapes=[pltpu.VMEM((S + FFN_PAD, LANES), F32)] * 4 + [pltpu.VMEM((SUBLANES, LANES), F32)] * 2
        + [pltpu.VMEM((1, LANES), F32)] * 2,
        compiler_params=_params(("parallel",)),
    )(u, u, w, w, b, b, df)


def _adamw_call(w, g, m, v, name):
    R, C = w.shape
    tr = _row_tile(R, C)
    c1 = 1.0 / (1.0 - ADAM_B1 ** ADAM_STEP)
    c2 = 1.0 / (1.0 - ADAM_B2 ** ADAM_STEP)

    def body(w_ref, g_ref, m_ref, v_ref, d_ref, mo_ref, vo_ref):
        gv = g_ref[...]
        mn = ADAM_B1 * m_ref[...] + (1.0 - ADAM_B1) * gv
        vn = ADAM_B2 * v_ref[...] + (1.0 - ADAM_B2) * (gv * gv)
        mo_ref[...] = mn
        vo_ref[...] = vn
        d_ref[...] = -ADAM_LR * ((mn * c1) / (jnp.sqrt(vn * c2) + ADAM_EPS) + ADAM_WD * w_ref[...])

    spec = pl.BlockSpec((tr, C), lambda i: (i, 0))
    return pl.pallas_call(
        body, name=name, grid=(R // tr,),
        in_specs=[spec] * 4, out_specs=[spec] * 3,
        out_shape=[jax.ShapeDtypeStruct((R, C), F32)] * 3,
        compiler_params=_params(("parallel",)),
    )(w, g, m, v)


def _position():
    return lax.axis_index("x"), lax.axis_index("y"), lax.axis_index("c")


def _chip_peers(x, y):
    return [(x, 1 - y), (1 - x, y), (1 - x, 1 - y)]


def _half_rows(ref, core, rows):
    h = rows // 2
    start = pl.multiple_of(core * h, 16)
    return ref.at[pl.ds(start, h), :] if len(ref.shape) == 2 else ref.at[:, pl.ds(start, h), :]


def _shard_half(ref, shard, core, rows):
    h = rows // 2
    return ref.at[shard, pl.ds(pl.multiple_of(core * h, 16), h), :]


ANY = pl.BlockSpec(memory_space=pl.ANY)


def _allgather_call(shards, whole):
    n, nw = len(shards), len(whole)
    outs_shape = [jax.ShapeDtypeStruct((N_CHIPS,) + s.shape, s.dtype) for s in shards + whole]

    def body(*refs):
        ins, outs = refs[:n + nw], refs[n + nw:2 * (n + nw)]
        send_sems, recv_sems, pass_send, pass_recv, own_send, own_recv = refs[2 * (n + nw):]
        x, y, c = _position()
        chip = 2 * x + y
        peers = _chip_peers(x, y)
        sent, local = [], []
        for i in range(n + nw):
            cp = pltpu.make_async_remote_copy(src_ref=ins[i], dst_ref=outs[i].at[chip], send_sem=own_send.at[i],
                                              recv_sem=own_recv.at[i], device_id=(x, y, 1 - c), device_id_type=MESH)
            cp.start()
            local.append(cp)
            rows = ins[i].shape[0]
            for k, (px, py) in enumerate(peers):
                if i < n:
                    src, dst = _half_rows(ins[i], c, rows), _shard_half(outs[i], chip, c, rows)
                else:
                    src, dst = ins[i], outs[i].at[chip]
                cp = pltpu.make_async_remote_copy(src_ref=src, dst_ref=dst, send_sem=send_sems.at[i, k],
                                                  recv_sem=recv_sems.at[i, k], device_id=(px, py, c), device_id_type=MESH)
                cp.start()
                sent.append(cp)
        passed = []
        for i in range(n + nw):
            rows = ins[i].shape[0]
            for k, (px, py) in enumerate(peers):
                landed = _shard_half(outs[i], 2 * px + py, c, rows) if i < n else outs[i].at[2 * px + py]
                pltpu.make_async_remote_copy(src_ref=landed, dst_ref=landed, send_sem=send_sems.at[i, k],
                                             recv_sem=recv_sems.at[i, k], device_id=(px, py, c), device_id_type=MESH).wait_recv()
                if i < n:
                    cp = pltpu.make_async_remote_copy(src_ref=landed, dst_ref=landed, send_sem=pass_send.at[i, k],
                                                      recv_sem=pass_recv.at[i, k], device_id=(x, y, 1 - c), device_id_type=MESH)
                    cp.start()
                    passed.append(cp)
        for cp in sent:
            cp.wait_send()
        for cp in passed:
            cp.wait()
        for cp in local:
            cp.wait()

    return pl.pallas_call(
        body, name="weight_allgather",
        in_specs=[ANY] * (n + nw), out_specs=[ANY] * (n + nw), out_shape=outs_shape,
        scratch_shapes=[pltpu.SemaphoreType.DMA((n + nw, 3)), pltpu.SemaphoreType.DMA((n + nw, 3)),
                        pltpu.SemaphoreType.DMA((n, 3)), pltpu.SemaphoreType.DMA((n, 3)),
                        pltpu.SemaphoreType.DMA((n + nw,)), pltpu.SemaphoreType.DMA((n + nw,))],
    )(*shards, *whole)


def _sibling_exchange_call(grads):
    n = len(grads)
    halves = [jax.ShapeDtypeStruct((N_CHIPS, g.shape[1] // 2, g.shape[2]), F32) for g in grads]

    def body(*refs):
        ins, outs = refs[:n], refs[n:2 * n]
        send_sems, recv_sems = refs[2 * n:]
        x, y, c = _position()
        copies = []
        for i in range(n):
            cp = pltpu.make_async_remote_copy(src_ref=_half_rows(ins[i], 1 - c, grads[i].shape[1]), dst_ref=outs[i],
                                              send_sem=send_sems.at[i], recv_sem=recv_sems.at[i],
                                              device_id=(x, y, 1 - c), device_id_type=MESH)
            cp.start()
            copies.append(cp)
        for cp in copies:
            cp.wait()

    return pl.pallas_call(
        body, name="grad_sibling_exchange",
        in_specs=[ANY] * n, out_specs=[ANY] * n, out_shape=halves,
        scratch_shapes=[pltpu.SemaphoreType.DMA((n,)), pltpu.SemaphoreType.DMA((n,))],
    )(*grads)


def _pair_sum_call(grad, recv, core, name):
    _, h, B = recv.shape
    tr = _row_tile(h, B)

    def body(core_ref, g_ref, r_ref, o_ref, ob_ref):
        s = g_ref[...] + r_ref[...]
        o_ref[...] = s
        ob_ref[...] = s.astype(BF16)

    g_spec = pl.BlockSpec((None, tr, B), lambda q, i, core_ref: (q, core_ref[0] * (h // tr) + i, 0))
    spec = pl.BlockSpec((None, tr, B), lambda q, i, core_ref: (q, i, 0))
    return pl.pallas_call(
        body, name=name,
        grid_spec=pltpu.PrefetchScalarGridSpec(num_scalar_prefetch=1, grid=(N_CHIPS, h // tr), in_specs=[g_spec, spec],
                                               out_specs=[spec, spec]),
        out_shape=[jax.ShapeDtypeStruct(recv.shape, F32), jax.ShapeDtypeStruct(recv.shape, BF16)],
        compiler_params=_params(("parallel", "parallel")),
    )(core, grad, recv)


def _chip_exchange_call(partials):
    n = len(partials)
    outs_shape = [jax.ShapeDtypeStruct((3,) + t.shape[1:], BF16) for t in partials]

    def body(*refs):
        ins, outs = refs[:n], refs[n:2 * n]
        send_sems, recv_sems = refs[2 * n:]
        x, y, c = _position()
        copies = []
        for i in range(n):
            for k, (px, py) in enumerate(_chip_peers(x, y)):
                cp = pltpu.make_async_remote_copy(src_ref=ins[i].at[2 * px + py], dst_ref=outs[i].at[k], send_sem=send_sems.at[i, k],
                                                  recv_sem=recv_sems.at[i, k], device_id=(px, py, c), device_id_type=MESH)
                cp.start()
                copies.append(cp)
        for cp in copies:
            cp.wait()

    return pl.pallas_call(
        body, name="grad_chip_exchange",
        in_specs=[ANY] * n, out_specs=[ANY] * n, out_shape=outs_shape,
        scratch_shapes=[pltpu.SemaphoreType.DMA((n, 3)), pltpu.SemaphoreType.DMA((n, 3))],
    )(*partials)


def _chip_sum_call(partial, recv, chip_core, name):
    _, h, B = recv.shape
    tr = _row_tile(h, B)

    def body(cc_ref, p_ref, r_ref, o_ref):
        o_ref[...] = ((p_ref[...] + r_ref[0].astype(F32)) + r_ref[1].astype(F32)) + r_ref[2].astype(F32)

    return pl.pallas_call(
        body, name=name,
        grid_spec=pltpu.PrefetchScalarGridSpec(
            num_scalar_prefetch=1, grid=(h // tr,),
            in_specs=[pl.BlockSpec((None, tr, B), lambda i, cc_ref: (cc_ref[0], i, 0)),
                      pl.BlockSpec((3, tr, B), lambda i, cc_ref: (0, i, 0))],
            out_specs=pl.BlockSpec((tr, B), lambda i, cc_ref: (cc_ref[1] * (h // tr) + i, 0))),
        out_shape=jax.ShapeDtypeStruct((2 * h, B), F32),
        compiler_params=_params(("parallel",)),
    )(chip_core, partial, recv)


def _sibling_assemble_call(shards):
    n = len(shards)

    def body(*refs):
        ins, outs = refs[:n], refs[n:2 * n]
        send_sems, recv_sems = refs[2 * n:]
        x, y, c = _position()
        copies = []
        for i in range(n):
            rows = shards[i].shape[0]
            cp = pltpu.make_async_remote_copy(src_ref=_half_rows(ins[i], c, rows), dst_ref=_half_rows(outs[i], c, rows),
                                              send_sem=send_sems.at[i], recv_sem=recv_sems.at[i],
                                              device_id=(x, y, 1 - c), device_id_type=MESH)
            cp.start()
            copies.append(cp)
        for cp in copies:
            cp.wait()

    return pl.pallas_call(
        body, name="grad_sibling_assemble",
        in_specs=[ANY] * n, out_specs=[ANY] * n,
        out_shape=[jax.ShapeDtypeStruct(s.shape, F32) for s in shards],
        input_output_aliases={i: i for i in range(n)},
        scratch_shapes=[pltpu.SemaphoreType.DMA((n,)), pltpu.SemaphoreType.DMA((n,))],
    )(*shards)


def _small_allreduce_call(packed):
    rows = packed.shape[0]

    def body(x_ref, o_ref, buf_ref, send_sems, recv_sems):
        x, y, c = _position()
        me = 4 * x + 2 * y + c
        buf_ref[me] = x_ref[...]
        copies = []
        for k in range(1, 8):
            peer = (1 - x if k & 4 else x, 1 - y if k & 2 else y, 1 - c if k & 1 else c)
            cp = pltpu.make_async_remote_copy(src_ref=buf_ref.at[me], dst_ref=buf_ref.at[me], send_sem=send_sems.at[k - 1],
                                              recv_sem=recv_sems.at[k - 1], device_id=peer, device_id_type=MESH)
            cp.start()
            copies.append(cp)
        for cp in copies:
            cp.wait()
        acc = buf_ref[0]
        for d in range(1, 8):
            acc = acc + buf_ref[d]
        o_ref[...] = acc

    return pl.pallas_call(
        body, name="small_grad_allreduce",
        in_specs=[pl.BlockSpec(memory_space=pltpu.VMEM)], out_specs=pl.BlockSpec(memory_space=pltpu.VMEM),
        out_shape=jax.ShapeDtypeStruct((rows, LANES), F32),
        scratch_shapes=[pltpu.VMEM((8, rows, LANES), F32), pltpu.SemaphoreType.DMA((7,)), pltpu.SemaphoreType.DMA((7,))],
    )(packed)


def _pack(arrays):
    flat = jnp.concatenate([a.reshape(-1).astype(F32) for a in arrays])
    rows = -(-flat.shape[0] // LANES)
    rows = -(-rows // SUBLANES) * SUBLANES
    flat = jnp.pad(flat, (0, rows * LANES - flat.shape[0]))
    return flat.reshape(rows, LANES)


def _unpack(packed, shapes):
    flat = packed.reshape(-1)
    out, off = [], 0
    for shp in shapes:
        size = int(np.prod(shp))
        out.append(flat[off:off + size].reshape(shp))
        off += size
    return out


def _local_step(xs, target, P):
    S, D = xs.shape
    qkv_width = 3 * N_HEADS * HEAD_DIM
    glu_col0, gate_col0 = qkv_width, qkv_width + 2 * D
    shard_major = lambda g: g.reshape(N_CHIPS, g.shape[0] // N_CHIPS, g.shape[1])

    h1 = _rms_fwd_call(xs, P["norm_mix_pre"])
    proj = _matmul(h1, P["w_in"], "nn", "proj_in")
    buckets = _bucket_tables()
    bias = _bias_table_call(P["rel_bias"], buckets)
    qkv, parts = [], []
    for g in range(N_GROUPS):
        q, k, v = [_to_group_order(proj, 3 * t + g, g, f"group_order_{'qkv'[t]}{g}") for t in range(3)]
        qkv.append((q, k, v))
        o_g, lse_g = _attn_fwd_call(q, k, v, bias, g)
        parts += [_to_token_order(o_g, g, f"token_order_o{g}"), _to_token_order(lse_g, g, f"token_order_lse{g}")]
    a, a_bf, lse = _attn_merge_call(parts)
    y_a = _matmul(a_bf, P["w_attn_out"], "nn", "attn_out")
    c1 = _conv_fwd_call(proj, glu_col0, P["conv_dw_w"], P["conv_dw_b"])
    cact = _ln_silu_call(c1, P["conv_ln_g"], P["conv_ln_b"])
    y_c = _matmul(cact, P["conv_pw_w"], "nn", "conv_pw")
    mixed = _mix_call(proj, gate_col0, P["b_gate"], y_a, y_c)
    out = _matmul(mixed, P["w_out"], "nn", "mix_out")
    x1, h2 = _res1_call(xs, out, P["norm_mix_post"], P["norm_ffn_pre"])
    u = _matmul(h2, P["w_up"], "nn", "ffn_up")
    f = _ffn_fwd_call(u, P["ffn_conv_w"], P["ffn_conv_b"])
    yff = _matmul(f, P["w_down"], "nn", "ffn_down")
    loss_tile, dx2 = _loss_call(yff, x1, P["norm_ffn_post"], target)

    G = {}
    dyff, G["norm_ffn_post"] = _rms_bwd_call(yff, P["norm_ffn_post"], dx2, "rms_ffn_post_bwd")
    df = _matmul(dyff, P["w_down"], "nt", "ffn_down_dx")
    G["w_down"] = shard_major(_matmul(f, dyff, "tn", "ffn_down_dw"))
    dug, duv, dwg, dwv, dbg, dbv = _ffn_bwd_call(u, P["ffn_conv_w"], P["ffn_conv_b"], df)
    G["ffn_conv_w"] = jnp.concatenate([dwg[:FFN_CONV_WIDTH], dwv[:FFN_CONV_WIDTH]], axis=1)
    G["ffn_conv_b"] = jnp.concatenate([dbg, dbv], axis=1)
    du = jnp.concatenate([dug, duv], axis=1)
    dh2 = _matmul(du, P["w_up"], "nt", "ffn_up_dx")
    G["w_up"] = _matmul(h2, du, "tn", "ffn_up_dw", out_shards=True)
    dx1, dout, G["norm_ffn_pre"], G["norm_mix_post"] = _mid_bwd_call(x1, P["norm_ffn_pre"], dh2, dx2, out, P["norm_mix_post"])
    dmixed = _matmul(dout, P["w_out"], "nt", "mix_out_dx")
    G["w_out"] = shard_major(_matmul(mixed, dout, "tn", "mix_out_dw"))
    dya, dyc, dga, dgc, dba, dbc = _mix_bwd_call(dmixed, proj, gate_col0, P["b_gate"], y_a, y_c)
    G["b_gate"] = jnp.concatenate([dba, dbc], axis=1)
    da = _matmul(dya, P["w_attn_out"], "nt", "attn_out_dx")
    G["w_attn_out"] = _matmul(a_bf, dya, "tn", "attn_out_dw", out_shards=True)
    dcact = _matmul(dyc, P["conv_pw_w"], "nt", "conv_pw_dx")
    G["conv_pw_w"] = shard_major(_matmul(cact, dyc, "tn", "conv_pw_dw"))
    dc1, G["conv_ln_g"], G["conv_ln_b"] = _ln_silu_bwd_call(c1, P["conv_ln_g"], P["conv_ln_b"], dcact)
    dval, dgate, dw_dw, G["conv_dw_b"] = _conv_bwd_call(proj, glu_col0, P["conv_dw_w"], dc1)
    G["conv_dw_w"] = dw_dw[:CONV_WIDTH]
    delta = _attn_delta_call(a, da)
    dqs, dks, dvs, dbs = [], [], [], []
    for g in range(N_GROUPS):
        grouped = [_to_group_order(t, 0, g, f"group_order_{n}{g}")[0] for t, n in ((da, "da"), (lse, "lse"), (delta, "delta"))]
        dq, dk, dv, db = _attn_bwd_call(*qkv[g], bias, *grouped, g)
        dqs.append(_to_token_order(dq, g, f"token_order_dq{g}"))
        dks.append(_to_token_order(dk, g, f"token_order_dk{g}"))
        dvs.append(_to_token_order(dv, g, f"token_order_dv{g}"))
        dbs.append(db)
    G["rel_bias"] = _bias_grad_call(jnp.concatenate(dbs, axis=0), buckets)
    dproj = jnp.concatenate([t.astype(BF16) for t in dqs + dks + dvs] + [dval, dgate, dga, dgc], axis=1)
    dh1 = _matmul(dproj, P["w_in"], "nt", "proj_in_dx")
    G["w_in"] = _matmul(h1, dproj, "tn", "proj_in_dw", out_shards=True, tm=512)
    grad_x, G["norm_mix_pre"] = _in_bwd_call(xs, P["norm_mix_pre"], dh1, dx1)
    return loss_tile, grad_x, G


def kernel(x, w_in, b_gate, rel_bias, w_attn_out, conv_dw_w, conv_dw_b, conv_ln_g, conv_ln_b, conv_pw_w, w_out, norm_mix_pre, norm_mix_post, norm_ffn_pre, norm_ffn_post, w_up, ffn_conv_w, ffn_conv_b, w_down, loss_target, m_w_in, m_b_gate, m_rel_bias, m_w_attn_out, m_conv_dw_w, m_conv_dw_b, m_conv_ln_g, m_conv_ln_b, m_conv_pw_w, m_w_out, m_norm_mix_pre, m_norm_mix_post, m_norm_ffn_pre, m_norm_ffn_post, m_w_up, m_ffn_conv_w, m_ffn_conv_b, m_w_down, v_w_in, v_b_gate, v_rel_bias, v_w_attn_out, v_conv_dw_w, v_conv_dw_b, v_conv_ln_g, v_conv_ln_b, v_conv_pw_w, v_w_out, v_norm_mix_pre, v_norm_mix_post, v_norm_ffn_pre, v_norm_ffn_post, v_w_up, v_ffn_conv_w, v_ffn_conv_b, v_w_down):
    weights = dict(w_in=w_in, b_gate=b_gate, rel_bias=rel_bias, w_attn_out=w_attn_out, conv_dw_w=conv_dw_w, conv_dw_b=conv_dw_b,
                   conv_ln_g=conv_ln_g, conv_ln_b=conv_ln_b, conv_pw_w=conv_pw_w, w_out=w_out, norm_mix_pre=norm_mix_pre,
                   norm_mix_post=norm_mix_post, norm_ffn_pre=norm_ffn_pre, norm_ffn_post=norm_ffn_post, w_up=w_up,
                   ffn_conv_w=ffn_conv_w, ffn_conv_b=ffn_conv_b, w_down=w_down)
    m_in = dict(w_in=m_w_in, b_gate=m_b_gate, rel_bias=m_rel_bias, w_attn_out=m_w_attn_out, conv_dw_w=m_conv_dw_w,
                conv_dw_b=m_conv_dw_b, conv_ln_g=m_conv_ln_g, conv_ln_b=m_conv_ln_b, conv_pw_w=m_conv_pw_w, w_out=m_w_out,
                norm_mix_pre=m_norm_mix_pre, norm_mix_post=m_norm_mix_post, norm_ffn_pre=m_norm_ffn_pre,
                norm_ffn_post=m_norm_ffn_post, w_up=m_w_up, ffn_conv_w=m_ffn_conv_w, ffn_conv_b=m_ffn_conv_b, w_down=m_w_down)
    v_in = dict(w_in=v_w_in, b_gate=v_b_gate, rel_bias=v_rel_bias, w_attn_out=v_w_attn_out, conv_dw_w=v_conv_dw_w,
                conv_dw_b=v_conv_dw_b, conv_ln_g=v_conv_ln_g, conv_ln_b=v_conv_ln_b, conv_pw_w=v_conv_pw_w, w_out=v_w_out,
                norm_mix_pre=v_norm_mix_pre, norm_mix_post=v_norm_mix_post, norm_ffn_pre=v_norm_ffn_pre,
                norm_ffn_post=v_norm_ffn_post, w_up=v_w_up, ffn_conv_w=v_ffn_conv_w, ffn_conv_b=v_ffn_conv_b, w_down=v_w_down)
    names = list(weights)
    xi, yi, ci = _position()
    chip = 2 * xi + yi
    core_arr = jnp.reshape(ci, (1,)).astype(jnp.int32)

    xs = x[0]
    target = loss_target[0]
    S, D = xs.shape

    big = ["w_in", "w_attn_out", "conv_pw_w", "w_out", "w_up", "w_down"]
    row_sharded = ("conv_pw_w", "w_out", "w_down")
    gathered = _allgather_call([weights[k][0].astype(BF16) for k in big], [conv_dw_w[0], ffn_conv_w[0]])
    W = {k: (g.reshape(-1, g.shape[2]) if k in row_sharded else g) for k, g in zip(big, gathered[:6])}
    dw_full = jnp.concatenate(list(gathered[6]), axis=1)
    fc_full = jnp.concatenate(list(gathered[7]), axis=1)
    P = dict(W, conv_dw_w=dw_full, ffn_conv_w=fc_full, b_gate=b_gate, rel_bias=rel_bias, conv_dw_b=conv_dw_b,
             conv_ln_g=conv_ln_g, conv_ln_b=conv_ln_b, norm_mix_pre=norm_mix_pre, norm_mix_post=norm_mix_post,
             norm_ffn_pre=norm_ffn_pre, norm_ffn_post=norm_ffn_post, ffn_conv_b=ffn_conv_b)
    loss_tile, grad_x, G = _local_step(xs, target, P)
    loss = lax.psum(loss_tile[0, 0], ("x", "y", "c"))

    recv1 = _sibling_exchange_call([G[k] for k in big])
    pair_f32, pair_bf16 = [], []
    for k, r1 in zip(big, recv1):
        s32, s16 = _pair_sum_call(G[k], r1, core_arr, f"pair_sum_{k}")
        pair_f32.append(s32)
        pair_bf16.append(s16)
    recv2 = _chip_exchange_call(pair_bf16)
    chip_core = jnp.stack([chip, ci]).astype(jnp.int32)
    halves = [_chip_sum_call(p, r2, chip_core, f"chip_sum_{k}") for k, p, r2 in zip(big, pair_f32, recv2)]
    reduced = dict(zip(big, _sibling_assemble_call(halves)))

    small = [k for k in names if k not in big]
    packed = _pack([G[k] for k in small])
    summed = _unpack(_small_allreduce_call(packed), [G[k].shape for k in small])
    for k, gsum in zip(small, summed):
        if k in ("conv_dw_w", "ffn_conv_w"):
            cols = weights[k].shape[2]
            reduced[k] = lax.dynamic_slice_in_dim(gsum, chip * cols, cols, axis=1)
        else:
            reduced[k] = gsum

    grads, deltas, new_m, new_v = {}, {}, {}, {}
    for k in big:
        d, mn, vn = _adamw_call(weights[k][0], reduced[k], m_in[k][0], v_in[k][0], f"adamw_{k}")
        grads[k], deltas[k], new_m[k], new_v[k] = reduced[k][None], d[None], mn[None], vn[None]
    flat2 = lambda t: t.reshape(-1, t.shape[-1]) if t.ndim == 3 else t
    pw = _pack([flat2(weights[k]) for k in small])
    pg = _pack([reduced[k] for k in small])
    pm = _pack([flat2(m_in[k]) for k in small])
    pv = _pack([flat2(v_in[k]) for k in small])
    pd, pmn, pvn = _adamw_call(pw, pg, pm, pv, "adamw_small")
    shapes = [weights[k].shape for k in small]
    for k, gk, dk_, mk, vk in zip(small, [reduced[k] for k in small], _unpack(pd, shapes), _unpack(pmn, shapes), _unpack(pvn, shapes)):
        grads[k], deltas[k], new_m[k], new_v[k] = gk.reshape(weights[k].shape), dk_, mk, vk

    return (loss, grad_x[None], *[grads[k] for k in names], *[deltas[k] for k in names],
            *[new_m[k] for k in names], *[new_v[k] for k in names])
```

```python
import functools
import math

import jax
import jax.numpy as jnp
import numpy as np
from jax import lax
from jax.experimental import pallas as pl
from jax.experimental.pallas import tpu as pltpu

F32 = jnp.float32
BF16 = jnp.bfloat16
MESH = pl.DeviceIdType.MESH

HEAD_DIM = 128
HEADS_PER_GROUP = 4
DILATED_PATTERNS = ((128, 1), (512, 4), (2048, 16))
N_GROUPS = 3
N_HEADS = N_GROUPS * HEADS_PER_GROUP
SPAN = 128
GROUP_WIDTH = HEADS_PER_GROUP * HEAD_DIM
CONV_WIDTH = 31
FFN_CONV_WIDTH = 3
N_BUCKETS = 32
MAX_DISTANCE = 2048
RMS_EPS = 1e-6
LN_EPS = 1e-5
NEG_INF = -1e30
ADAM_LR = 0.001
ADAM_B1 = 0.9
ADAM_B2 = 0.999
ADAM_EPS = 1e-08
ADAM_WD = 0.01
ADAM_STEP = 10

LANES = 128
SUBLANES = 8
ROW_TILE = 256
TIME_BLOCK = 128
CONV_PAD = 32
FFN_PAD = 8
VMEM_LIMIT = 56 << 20


def _params(sem=None, vmem=None):
    kw = {}
    if sem is not None:
        kw["dimension_semantics"] = sem
    if vmem is not None:
        kw["vmem_limit_bytes"] = vmem
    return pltpu.CompilerParams(**kw)


def _pick(n, cands):
    for c in cands:
        if n % c == 0:
            return c
    return n


ELEMENTWISE_TILE_BYTES = 1 << 20


def _row_tile(rows, cols):
    for c in (512, 256, 128, 64, 32, 16):
        if rows % c == 0 and c * cols * 4 <= ELEMENTWISE_TILE_BYTES:
            return c
    return 16 if rows % 16 == 0 else 8


N_CHIPS = 4
M_TILES = (1024, 1408, 512, 256, 128)
N_TILES = (512, 1408, 256, 128)
K_TILES = (2176, 2048, 1408, 1024, 512, 256, 128)


def _matmul(a, b, mode, name, out_shards=False, tm=None):
    assert a.dtype == BF16 and b.dtype == BF16, (name, a.dtype, b.dtype)
    b3 = b.ndim == 3
    tn = tk = None
    if mode == "nn":
        M, K = a.shape
        N = b.shape[-1] * (N_CHIPS if b3 else 1)
        tn = b.shape[-1] if b3 else None
    elif mode == "nt":
        M, K = a.shape
        N = b.shape[-2]
        tk = b.shape[-1] if b3 else None
    else:
        K, M = a.shape
        N = b.shape[1]
        tn = N // N_CHIPS if out_shards else None
    tm = tm or _pick(M, M_TILES)
    tn = tn or _pick(N, N_TILES)
    tk = tk or _pick(K, K_TILES)
    nk = K // tk
    dn = {"nn": (((1,), (0,)), ((), ())), "nt": (((1,), (1,)), ((), ())), "tn": (((0,), (0,)), ((), ()))}[mode]

    def body(a_ref, b_ref, o_ref):
        if nk == 1:
            o_ref[...] = lax.dot_general(a_ref[...], b_ref[...], dn, preferred_element_type=F32)
        else:
            @pl.when(pl.program_id(2) == 0)
            def _():
                o_ref[...] = jnp.zeros_like(o_ref)

            o_ref[...] += lax.dot_general(a_ref[...], b_ref[...], dn, preferred_element_type=F32)

    if mode == "tn":
        a_spec = pl.BlockSpec((tk, tm), lambda i, j, k: (k, i))
    else:
        a_spec = pl.BlockSpec((tm, tk), lambda i, j, k: (i, k))
    if mode == "nn":
        b_spec = pl.BlockSpec((None, tk, tn), lambda i, j, k: (j, k, 0)) if b3 else pl.BlockSpec((tk, tn), lambda i, j, k: (k, j))
    elif mode == "nt":
        b_spec = pl.BlockSpec((None, tn, tk), lambda i, j, k: (k, j, 0)) if b3 else pl.BlockSpec((tn, tk), lambda i, j, k: (j, k))
    else:
        b_spec = pl.BlockSpec((tk, tn), lambda i, j, k: (k, j))
    if out_shards:
        out_spec = pl.BlockSpec((None, tm, tn), lambda i, j, k: (j, i, 0))
        out_shape = jax.ShapeDtypeStruct((N_CHIPS, M, tn), F32)
    else:
        out_spec = pl.BlockSpec((tm, tn), lambda i, j, k: (i, j))
        out_shape = jax.ShapeDtypeStruct((M, N), F32)
    return pl.pallas_call(
        body, name=name, grid=(M // tm, N // tn, nk),
        in_specs=[a_spec, b_spec], out_specs=out_spec, out_shape=out_shape,
        compiler_params=_params(("parallel", "parallel", "arbitrary"), VMEM_LIMIT),
    )(a, b)


def _rms(x, g):
    r = lax.rsqrt(jnp.mean(x * x, axis=-1, keepdims=True) + RMS_EPS)
    return x * r * g


def _rms_bwd(x, g, dy):
    r = lax.rsqrt(jnp.mean(x * x, axis=-1, keepdims=True) + RMS_EPS)
    n = x * r
    dn = dy * g
    dx = r * (dn - n * jnp.mean(dn * n, axis=-1, keepdims=True))
    return dx, jnp.sum(dy * n, axis=0, keepdims=True)


def _sigmoid(x):
    return 1.0 / (1.0 + jnp.exp(-x))


_GELU_C = math.sqrt(2.0 / math.pi)


def _gelu(x):
    return 0.5 * x * (1.0 + jnp.tanh(_GELU_C * (x + 0.044715 * x * x * x)))


def _gelu_grad(x):
    t = jnp.tanh(_GELU_C * (x + 0.044715 * x * x * x))
    return 0.5 * (1.0 + t) + 0.5 * x * (1.0 - t * t) * _GELU_C * (1.0 + 3.0 * 0.044715 * x * x)


def _row_spec(width, col_block=0):
    return pl.BlockSpec((ROW_TILE, width), lambda i: (i, col_block))


def _vec_spec(width, col_block=0):
    return pl.BlockSpec((1, width), lambda i: (0, col_block))


def _accumulate(ref, part):
    @pl.when(pl.program_id(0) == 0)
    def _():
        ref[...] = part

    @pl.when(pl.program_id(0) > 0)
    def _():
        ref[...] += part


def _rms_fwd_call(x, g):
    S, D = x.shape

    def body(x_ref, g_ref, h_ref):
        h_ref[...] = _rms(x_ref[...], g_ref[...]).astype(BF16)

    return pl.pallas_call(
        body, name="rms_mix_pre", grid=(S // ROW_TILE,),
        in_specs=[_row_spec(D), _vec_spec(D)], out_specs=_row_spec(D),
        out_shape=jax.ShapeDtypeStruct((S, D), BF16),
        compiler_params=_params(("parallel",)),
    )(x, g)


def _ln_silu_call(c1, g, b):
    S, C = c1.shape

    def body(c_ref, g_ref, b_ref, o_ref):
        xv = c_ref[...]
        mu = jnp.mean(xv, axis=-1, keepdims=True)
        xc = xv - mu
        var = jnp.mean(xc * xc, axis=-1, keepdims=True)
        z = xc * lax.rsqrt(var + LN_EPS) * g_ref[...] + b_ref[...]
        o_ref[...] = (z * _sigmoid(z)).astype(BF16)

    return pl.pallas_call(
        body, name="conv_ln_silu", grid=(S // ROW_TILE,),
        in_specs=[_row_spec(C), _vec_spec(C), _vec_spec(C)], out_specs=_row_spec(C),
        out_shape=jax.ShapeDtypeStruct((S, C), BF16),
        compiler_params=_params(("parallel",)),
    )(c1, g, b)


def _ln_silu_bwd_call(c1, g, b, dc):
    S, C = c1.shape

    def body(c_ref, g_ref, b_ref, dc_ref, dx_ref, dg_ref, db_ref):
        xv = c_ref[...]
        mu = jnp.mean(xv, axis=-1, keepdims=True)
        xc = xv - mu
        rs = lax.rsqrt(jnp.mean(xc * xc, axis=-1, keepdims=True) + LN_EPS)
        xh = xc * rs
        z = xh * g_ref[...] + b_ref[...]
        sg = _sigmoid(z)
        dz = dc_ref[...] * (sg * (1.0 + z * (1.0 - sg)))
        dxh = dz * g_ref[...]
        dx_ref[...] = rs * (dxh - jnp.mean(dxh, axis=-1, keepdims=True) - xh * jnp.mean(dxh * xh, axis=-1, keepdims=True))
        _accumulate(dg_ref, jnp.sum(dz * xh, axis=0, keepdims=True))
        _accumulate(db_ref, jnp.sum(dz, axis=0, keepdims=True))

    return pl.pallas_call(
        body, name="conv_ln_silu_bwd", grid=(S // ROW_TILE,),
        in_specs=[_row_spec(C), _vec_spec(C), _vec_spec(C), _row_spec(C)],
        out_specs=[_row_spec(C), _vec_spec(C), _vec_spec(C)],
        out_shape=[jax.ShapeDtypeStruct((S, C), F32), jax.ShapeDtypeStruct((1, C), F32), jax.ShapeDtypeStruct((1, C), F32)],
        compiler_params=_params(("arbitrary",)),
    )(c1, g, b, dc)


def _mix_call(proj, gate_col0, b_gate, y_a, y_c):
    S, D = y_a.shape
    w = 512
    nc = D // w
    ga0, gc0 = gate_col0 // w, (gate_col0 + D) // w

    def body(ga_ref, gc_ref, ba_ref, bc_ref, ya_ref, yc_ref, o_ref):
        o_ref[...] = (_sigmoid(ga_ref[...] + ba_ref[...]) * ya_ref[...]
                      + _sigmoid(gc_ref[...] + bc_ref[...]) * yc_ref[...]).astype(BF16)

    tile = lambda off: pl.BlockSpec((ROW_TILE, w), lambda i, j: (i, off + j))
    vec = lambda off: pl.BlockSpec((1, w), lambda i, j: (0, off + j))
    return pl.pallas_call(
        body, name="gate_mix", grid=(S // ROW_TILE, nc),
        in_specs=[tile(ga0), tile(gc0), vec(0), vec(nc), tile(0), tile(0)],
        out_specs=tile(0), out_shape=jax.ShapeDtypeStruct((S, D), BF16),
        compiler_params=_params(("parallel", "parallel")),
    )(proj, proj, b_gate, b_gate, y_a, y_c)


def _mix_bwd_call(dmixed, proj, gate_col0, b_gate, y_a, y_c):
    S, D = y_a.shape
    w = 512
    nc = D // w
    ga0, gc0 = gate_col0 // w, (gate_col0 + D) // w

    def body(dm_ref, ga_ref, gc_ref, ba_ref, bc_ref, ya_ref, yc_ref, dya_ref, dyc_ref, dga_ref, dgc_ref, dba_ref, dbc_ref):
        dm = dm_ref[...]
        sa = _sigmoid(ga_ref[...] + ba_ref[...])
        sc = _sigmoid(gc_ref[...] + bc_ref[...])
        dya_ref[...] = (dm * sa).astype(BF16)
        dyc_ref[...] = (dm * sc).astype(BF16)
        dga = dm * ya_ref[...] * sa * (1.0 - sa)
        dgc = dm * yc_ref[...] * sc * (1.0 - sc)
        dga_ref[...] = dga.astype(BF16)
        dgc_ref[...] = dgc.astype(BF16)
        pa = jnp.sum(dga, axis=0, keepdims=True)
        pc = jnp.sum(dgc, axis=0, keepdims=True)

        @pl.when(pl.program_id(1) == 0)
        def _():
            dba_ref[...] = pa
            dbc_ref[...] = pc

        @pl.when(pl.program_id(1) > 0)
        def _():
            dba_ref[...] += pa
            dbc_ref[...] += pc

    tile = lambda off: pl.BlockSpec((ROW_TILE, w), lambda j, i: (i, off + j))
    vec = lambda off: pl.BlockSpec((1, w), lambda j, i: (0, off + j))
    return pl.pallas_call(
        body, name="gate_mix_bwd", grid=(nc, S // ROW_TILE),
        in_specs=[tile(0), tile(ga0), tile(gc0), vec(0), vec(nc), tile(0), tile(0)],
        out_specs=[tile(0), tile(0), tile(0), tile(0), vec(0), vec(0)],
        out_shape=[jax.ShapeDtypeStruct((S, D), BF16)] * 4 + [
                   jax.ShapeDtypeStruct((1, D), F32), jax.ShapeDtypeStruct((1, D), F32)],
        compiler_params=_params(("parallel", "arbitrary")),
    )(dmixed, proj, proj, b_gate, b_gate, y_a, y_c)


def _res1_call(x, out, g_post, g_pre):
    S, D = x.shape

    def body(x_ref, o_ref, gp_ref, gq_ref, x1_ref, h2_ref):
        x1 = x_ref[...] + _rms(o_ref[...], gp_ref[...])
        x1_ref[...] = x1
        h2_ref[...] = _rms(x1, gq_ref[...]).astype(BF16)

    return pl.pallas_call(
        body, name="residual_mix", grid=(S // ROW_TILE,),
        in_specs=[_row_spec(D), _row_spec(D), _vec_spec(D), _vec_spec(D)],
        out_specs=[_row_spec(D), _row_spec(D)],
        out_shape=[jax.ShapeDtypeStruct((S, D), F32), jax.ShapeDtypeStruct((S, D), BF16)],
        compiler_params=_params(("parallel",)),
    )(x, out, g_post, g_pre)


def _loss_call(y, x1, g_post, target):
    S, D = y.shape

    def body(y_ref, x1_ref, g_ref, t_ref, loss_ref, dx_ref):
        err = x1_ref[...] + _rms(y_ref[...], g_ref[...]) - t_ref[...]
        dx_ref[...] = err * (1.0 / D)
        part = 0.5 * jnp.sum(jnp.mean(err * err, axis=-1, keepdims=True), axis=0, keepdims=True)
        _accumulate(loss_ref, jnp.broadcast_to(part, (SUBLANES, LANES)))

    return pl.pallas_call(
        body, name="residual_ffn_loss", grid=(S // ROW_TILE,),
        in_specs=[_row_spec(D), _row_spec(D), _vec_spec(D), _row_spec(D)],
        out_specs=[pl.BlockSpec((SUBLANES, LANES), lambda i: (0, 0)), _row_spec(D)],
        out_shape=[jax.ShapeDtypeStruct((SUBLANES, LANES), F32), jax.ShapeDtypeStruct((S, D), F32)],
        compiler_params=_params(("arbitrary",)),
    )(y, x1, g_post, target)


def _rms_bwd_call(x, g, dy, name):
    S, D = x.shape

    def body(x_ref, g_ref, dy_ref, dx_ref, dg_ref):
        dx, dg = _rms_bwd(x_ref[...], g_ref[...], dy_ref[...])
        dx_ref[...] = dx.astype(BF16)
        _accumulate(dg_ref, dg)

    return pl.pallas_call(
        body, name=name, grid=(S // ROW_TILE,),
        in_specs=[_row_spec(D), _vec_spec(D), _row_spec(D)],
        out_specs=[_row_spec(D), _vec_spec(D)],
        out_shape=[jax.ShapeDtypeStruct((S, D), BF16), jax.ShapeDtypeStruct((1, D), F32)],
        compiler_params=_params(("arbitrary",)),
    )(x, g, dy)


def _mid_bwd_call(x1, g_pre, dh2, dx2, out, g_post):
    S, D = x1.shape

    def body(x1_ref, gq_ref, dh_ref, dx2_ref, o_ref, gp_ref, dx1_ref, do_ref, dgq_ref, dgp_ref):
        d, dgq = _rms_bwd(x1_ref[...], gq_ref[...], dh_ref[...])
        dx1 = dx2_ref[...] + d
        dx1_ref[...] = dx1
        do, dgp = _rms_bwd(o_ref[...], gp_ref[...], dx1)
        do_ref[...] = do.astype(BF16)
        _accumulate(dgq_ref, dgq)
        _accumulate(dgp_ref, dgp)

    return pl.pallas_call(
        body, name="residual_mix_bwd", grid=(S // ROW_TILE,),
        in_specs=[_row_spec(D), _vec_spec(D), _row_spec(D), _row_spec(D), _row_spec(D), _vec_spec(D)],
        out_specs=[_row_spec(D), _row_spec(D), _vec_spec(D), _vec_spec(D)],
        out_shape=[jax.ShapeDtypeStruct((S, D), F32), jax.ShapeDtypeStruct((S, D), BF16)] + [jax.ShapeDtypeStruct((1, D), F32)] * 2,
        compiler_params=_params(("arbitrary",)),
    )(x1, g_pre, dh2, dx2, out, g_post)


def _in_bwd_call(x, g, dh1, dx1):
    S, D = x.shape

    def body(x_ref, g_ref, dh_ref, dx1_ref, gx_ref, dg_ref):
        d, dg = _rms_bwd(x_ref[...], g_ref[...], dh_ref[...])
        gx_ref[...] = dx1_ref[...] + d
        _accumulate(dg_ref, dg)

    return pl.pallas_call(
        body, name="rms_mix_pre_bwd", grid=(S // ROW_TILE,),
        in_specs=[_row_spec(D), _vec_spec(D), _row_spec(D), _row_spec(D)],
        out_specs=[_row_spec(D), _vec_spec(D)],
        out_shape=[jax.ShapeDtypeStruct((S, D), F32), jax.ShapeDtypeStruct((1, D), F32)],
        compiler_params=_params(("arbitrary",)),
    )(x, g, dh1, dx1)


def _bucket_table(dilation):
    qi = np.arange(SPAN)[:, None]
    ki = np.arange(2 * SPAN)[None, :]
    dist = np.maximum(qi + SPAN - ki, 0) * dilation
    max_exact = N_BUCKETS // 2
    d = np.maximum(dist, 1).astype(np.float64)
    large = max_exact + (np.log(d / max_exact) / math.log(MAX_DISTANCE / max_exact) * (N_BUCKETS - max_exact)).astype(np.int32)
    large = np.minimum(large, N_BUCKETS - 1)
    return np.where(dist < max_exact, dist, large).astype(np.int32)


def _bucket_tables():
    return jnp.asarray(np.stack([_bucket_table(r) for _, r in DILATED_PATTERNS]))


def _bias_table_call(rel_bias, buckets):
    def body(rb_ref, bk_ref, o_ref):
        for h in range(N_HEADS):
            bk = bk_ref[h // HEADS_PER_GROUP]

            def step(b, acc):
                return jnp.where(bk == b, rb_ref[b, h], acc)

            o_ref[h] = lax.fori_loop(0, N_BUCKETS, step, jnp.zeros((SPAN, 2 * SPAN), F32))

    return pl.pallas_call(
        body, name="rel_bias_table",
        in_specs=[pl.BlockSpec(memory_space=pltpu.SMEM), pl.BlockSpec(memory_space=pltpu.VMEM)],
        out_specs=pl.BlockSpec(memory_space=pltpu.VMEM),
        out_shape=jax.ShapeDtypeStruct((N_HEADS, SPAN, 2 * SPAN), F32),
    )(rel_bias, buckets)


def _bias_grad_call(dbias, buckets):
    def body(db_ref, bk_ref, o_ref, rows_ref):
        for h in range(N_HEADS):
            bk = bk_ref[h // HEADS_PER_GROUP]
            dv = db_ref[h]

            def step(b, carry):
                rows_ref[h, b] = jnp.sum(jnp.where(bk == b, dv, 0.0), axis=0, keepdims=True)
                return carry

            lax.fori_loop(0, N_BUCKETS, step, 0)
        o_ref[...] = jnp.sum(rows_ref[...], axis=-1, keepdims=True)

    out = pl.pallas_call(
        body, name="rel_bias_grad",
        in_specs=[pl.BlockSpec(memory_space=pltpu.VMEM), pl.BlockSpec(memory_space=pltpu.VMEM)],
        out_specs=pl.BlockSpec(memory_space=pltpu.VMEM),
        out_shape=jax.ShapeDtypeStruct((N_HEADS, N_BUCKETS, 1, 1), F32),
        scratch_shapes=[pltpu.VMEM((N_HEADS, N_BUCKETS, 1, 2 * SPAN), F32)],
    )(dbias, buckets)
    return out.reshape(N_HEADS, N_BUCKETS).T


def _dot_nt(a, b):
    return lax.dot_general(a, b, (((1,), (1,)), ((), ())), preferred_element_type=F32)


def _dot_nn(a, b):
    return lax.dot_general(a, b, (((1,), (0,)), ((), ())), preferred_element_type=F32)


def _dot_tn(a, b):
    return lax.dot_general(a, b, (((0,), (0,)), ((), ())), preferred_element_type=F32)


def _band_masks(n, nb):
    qi = lax.broadcasted_iota(jnp.int32, (SPAN, SPAN), 0)
    ki = lax.broadcasted_iota(jnp.int32, (SPAN, SPAN), 1)
    prev_ok = jnp.logical_and(ki >= qi, n > 0)
    cur_ok = ki <= qi
    next_ok = jnp.logical_and(ki >= qi, n < nb - 1)
    return prev_ok, cur_ok, next_ok


def _regroup_call(src, col_block, r, inverse, name):
    S = src.shape[0]
    L = S // r
    nt = GROUP_WIDTH // LANES

    def body(x_ref, o_ref):
        for rho in range(r):
            if inverse:
                o_ref[pl.ds(rho, L, stride=r), :] = x_ref[rho * L:(rho + 1) * L, :]
            else:
                o_ref[rho * L:(rho + 1) * L, :] = x_ref[pl.ds(rho, L, stride=r), :]

    return pl.pallas_call(
        body, name=name, grid=(nt,),
        in_specs=[pl.BlockSpec((S, LANES), lambda i: (0, col_block * nt + i))],
        out_specs=pl.BlockSpec((S, LANES), lambda i: (0, i)),
        out_shape=jax.ShapeDtypeStruct((S, GROUP_WIDTH), F32),
        compiler_params=_params(("parallel",)),
    )(src)


def _to_group_order(src, col_block, group, name):
    r = DILATED_PATTERNS[group][1]
    if r == 1:
        return src, col_block
    return _regroup_call(src, col_block, r, False, name), 0


def _to_token_order(arr, group, name):
    r = DILATED_PATTERNS[group][1]
    return arr if r == 1 else _regroup_call(arr, 0, r, True, name)


def _attn_fwd_call(q, k, v, bias, group):
    S = q[0].shape[0]
    r = DILATED_PATTERNS[group][1]
    nb = S // r // SPAN
    scale = HEAD_DIM ** -0.5

    def body(q_ref, kp_ref, kc_ref, vp_ref, vc_ref, b_ref, o_ref, lse_ref):
        n = pl.program_id(1)
        prev_ok, cur_ok, _ = _band_masks(n, nb)
        for j in range(HEADS_PER_GROUP):
            sl = slice(j * HEAD_DIM, (j + 1) * HEAD_DIM)
            q = q_ref[:, sl].astype(BF16)
            sp = _dot_nt(q, kp_ref[:, sl].astype(BF16)) * scale + b_ref[j, :, :SPAN]
            sc = _dot_nt(q, kc_ref[:, sl].astype(BF16)) * scale + b_ref[j, :, SPAN:]
            sp = jnp.where(prev_ok, sp, NEG_INF)
            sc = jnp.where(cur_ok, sc, NEG_INF)
            m = jnp.maximum(jnp.max(sp, axis=-1, keepdims=True), jnp.max(sc, axis=-1, keepdims=True))
            pp = jnp.exp(sp - m)
            pc = jnp.exp(sc - m)
            den = jnp.sum(pp, axis=-1, keepdims=True) + jnp.sum(pc, axis=-1, keepdims=True)
            o_ref[:, sl] = (_dot_nn(pp.astype(BF16), vp_ref[:, sl].astype(BF16))
                            + _dot_nn(pc.astype(BF16), vc_ref[:, sl].astype(BF16))) / den
            lse_ref[:, sl] = jnp.broadcast_to(m + jnp.log(den), (SPAN, HEAD_DIM))

    blk = (SPAN, GROUP_WIDTH)
    cur = lambda cb: pl.BlockSpec(blk, lambda rho, n: (rho * nb + n, cb))
    prev = lambda cb: pl.BlockSpec(blk, lambda rho, n: (rho * nb + jnp.maximum(n - 1, 0), cb))
    return pl.pallas_call(
        body, name=f"attn_fwd_g{group}", grid=(r, nb),
        in_specs=[cur(q[1]), prev(k[1]), cur(k[1]), prev(v[1]), cur(v[1]),
                  pl.BlockSpec((HEADS_PER_GROUP, SPAN, 2 * SPAN), lambda rho, n: (group, 0, 0))],
        out_specs=[cur(0)] * 2,
        out_shape=[jax.ShapeDtypeStruct((S, GROUP_WIDTH), F32)] * 2,
        compiler_params=_params(("parallel", "parallel")),
    )(q[0], k[0], k[0], v[0], v[0], bias)


def _attn_merge_call(parts):
    S = parts[0].shape[0]

    def body(o1, s1, o2, s2, o3, s3, a_ref, ab_ref, lse_ref):
        mx = jnp.maximum(jnp.maximum(s1[...], s2[...]), s3[...])
        w1 = jnp.exp(s1[...] - mx)
        w2 = jnp.exp(s2[...] - mx)
        w3 = jnp.exp(s3[...] - mx)
        den = w1 + w2 + w3
        a = (w1 * o1[...] + w2 * o2[...] + w3 * o3[...]) / den
        a_ref[...] = a
        ab_ref[...] = a.astype(BF16)
        lse_ref[...] = mx + jnp.log(den)

    return pl.pallas_call(
        body, name="attn_merge", grid=(S // ROW_TILE,),
        in_specs=[_row_spec(GROUP_WIDTH)] * 6, out_specs=[_row_spec(GROUP_WIDTH)] * 3,
        out_shape=[jax.ShapeDtypeStruct((S, GROUP_WIDTH), F32), jax.ShapeDtypeStruct((S, GROUP_WIDTH), BF16),
                   jax.ShapeDtypeStruct((S, GROUP_WIDTH), F32)],
        compiler_params=_params(("parallel",)),
    )(*parts)


def _attn_delta_call(a, da):
    S = a.shape[0]

    def body(a_ref, da_ref, d_ref):
        for j in range(HEADS_PER_GROUP):
            sl = slice(j * HEAD_DIM, (j + 1) * HEAD_DIM)
            d = jnp.sum(a_ref[:, sl] * da_ref[:, sl], axis=-1, keepdims=True)
            d_ref[:, sl] = jnp.broadcast_to(d, (ROW_TILE, HEAD_DIM))

    return pl.pallas_call(
        body, name="attn_delta", grid=(S // ROW_TILE,),
        in_specs=[_row_spec(GROUP_WIDTH)] * 2, out_specs=_row_spec(GROUP_WIDTH),
        out_shape=jax.ShapeDtypeStruct((S, GROUP_WIDTH), F32),
        compiler_params=_params(("parallel",)),
    )(a, da)


def _attn_bwd_call(q, k, v, bias, da, lse, delta, group):
    S = q[0].shape[0]
    r = DILATED_PATTERNS[group][1]
    nb = S // r // SPAN
    scale = HEAD_DIM ** -0.5

    def body(q_ref, qn_ref, kp_ref, kc_ref, vp_ref, vc_ref, b_ref, da_ref, dan_ref, lse_ref, lsen_ref, dl_ref, dln_ref,
             dq_ref, dk_ref, dv_ref, db_ref):
        n = pl.program_id(1)
        prev_ok, cur_ok, next_ok = _band_masks(n, nb)
        first = jnp.logical_and(pl.program_id(0) == 0, n == 0)
        for j in range(HEADS_PER_GROUP):
            sl = slice(j * HEAD_DIM, (j + 1) * HEAD_DIM)
            q = q_ref[:, sl].astype(BF16)
            qn = qn_ref[:, sl].astype(BF16)
            kp = kp_ref[:, sl].astype(BF16)
            kc = kc_ref[:, sl].astype(BF16)
            vp = vp_ref[:, sl].astype(BF16)
            vc = vc_ref[:, sl].astype(BF16)
            dav = da_ref[:, sl].astype(BF16)
            dan = dan_ref[:, sl].astype(BF16)
            bp = b_ref[j, :, :SPAN]
            bc = b_ref[j, :, SPAN:]
            pp = jnp.exp(jnp.where(prev_ok, _dot_nt(q, kp) * scale + bp, NEG_INF) - lse_ref[:, sl])
            pc = jnp.exp(jnp.where(cur_ok, _dot_nt(q, kc) * scale + bc, NEG_INF) - lse_ref[:, sl])
            pn = jnp.exp(jnp.where(next_ok, _dot_nt(qn, kc) * scale + bp, NEG_INF) - lsen_ref[:, sl])
            dsp = pp * (_dot_nt(dav, vp) - dl_ref[:, sl])
            dsc = pc * (_dot_nt(dav, vc) - dl_ref[:, sl])
            dsn = pn * (_dot_nt(dan, vc) - dln_ref[:, sl])
            dsp_b, dsc_b, dsn_b = dsp.astype(BF16), dsc.astype(BF16), dsn.astype(BF16)
            dq_ref[:, sl] = (_dot_nn(dsp_b, kp) + _dot_nn(dsc_b, kc)) * scale
            dk_ref[:, sl] = (_dot_tn(dsc_b, q) + _dot_tn(dsn_b, qn)) * scale
            dv_ref[:, sl] = _dot_tn(pc.astype(BF16), dav) + _dot_tn(pn.astype(BF16), dan)

            @pl.when(first)
            def _():
                db_ref[j, :, :SPAN] = dsp
                db_ref[j, :, SPAN:] = dsc

            @pl.when(jnp.logical_not(first))
            def _():
                db_ref[j, :, :SPAN] += dsp
                db_ref[j, :, SPAN:] += dsc

    blk = (SPAN, GROUP_WIDTH)
    cur = lambda cb: pl.BlockSpec(blk, lambda rho, n: (rho * nb + n, cb))
    prev = lambda cb: pl.BlockSpec(blk, lambda rho, n: (rho * nb + jnp.maximum(n - 1, 0), cb))
    nxt = lambda cb: pl.BlockSpec(blk, lambda rho, n: (rho * nb + jnp.minimum(n + 1, nb - 1), cb))
    band = (HEADS_PER_GROUP, SPAN, 2 * SPAN)
    return pl.pallas_call(
        body, name=f"attn_bwd_g{group}", grid=(r, nb),
        in_specs=[cur(q[1]), nxt(q[1]), prev(k[1]), cur(k[1]), prev(v[1]), cur(v[1]),
                  pl.BlockSpec(band, lambda rho, n: (group, 0, 0)),
                  cur(0), nxt(0), cur(0), nxt(0), cur(0), nxt(0)],
        out_specs=[cur(0), cur(0), cur(0), pl.BlockSpec(band, lambda rho, n: (0, 0, 0))],
        out_shape=[jax.ShapeDtypeStruct((S, GROUP_WIDTH), F32)] * 3 + [jax.ShapeDtypeStruct(band, F32)],
        compiler_params=_params(("arbitrary", "arbitrary")),
    )(q[0], q[0], k[0], k[0], v[0], v[0], bias, da, da, lse, lse, delta, delta)


def _taps(width):
    return [(k, (width - 1 - k) // SUBLANES, (width - 1 - k) % SUBLANES) for k in range(width)]


def _shifted(win, width, pad, up):
    total = win.shape[0]
    for b in range(SUBLANES):
        taps = [(k, a) for k, a, bb in _taps(width) if bb == b]
        if not taps:
            continue
        if up:
            rolled = win if b == 0 else pltpu.roll(win, total - b, axis=0)
        else:
            rolled = win if b == 0 else pltpu.roll(win, b, axis=0)
        for k, a in taps:
            start = SUBLANES * a if up else pad - SUBLANES * a
            yield k, rolled[start:start + TIME_BLOCK, :]


def _conv_block(win, w_ref, width, pad):
    acc = None
    for k, rows in _shifted(win, width, pad, up=False):
        term = w_ref[k:k + 1, :] * rows
        acc = term if acc is None else acc + term
    return acc


def _conv_transpose_block(win, w_ref, width, pad):
    acc = None
    for k, rows in _shifted(win, width, pad, up=True):
        term = w_ref[k:k + 1, :] * rows
        acc = term if acc is None else acc + term
    return acc


def _conv_weight_grad(win, dy, dw_ref, width, pad):
    for k, rows in _shifted(win, width, pad, up=False):
        dw_ref[k:k + 1, :] += jnp.sum(dy * rows, axis=0, keepdims=True)


def _time_loop(S, step):
    def it(tb, carry):
        step(pl.multiple_of(tb * TIME_BLOCK, TIME_BLOCK))
        return carry

    lax.fori_loop(0, S // TIME_BLOCK, it, 0)


def _conv_fwd_call(proj, col0, w, b):
    S = proj.shape[0]
    C = w.shape[1]
    nt = C // LANES
    v0, g0 = col0 // LANES, (col0 + C) // LANES

    def body(val_ref, gate_ref, w_ref, b_ref, o_ref, pad_ref):
        pad_ref[0:CONV_PAD, :] = jnp.zeros((CONV_PAD, LANES), F32)
        pad_ref[CONV_PAD:, :] = val_ref[...] * _sigmoid(gate_ref[...])

        def step(t0):
            win = pad_ref[pl.ds(t0, TIME_BLOCK + CONV_PAD), :]
            o_ref[pl.ds(t0, TIME_BLOCK), :] = _conv_block(win, w_ref, CONV_WIDTH, CONV_PAD) + b_ref[...]

        _time_loop(S, step)

    seq = lambda off: pl.BlockSpec((S, LANES), lambda i: (0, off + i))
    return pl.pallas_call(
        body, name="conv_module", grid=(nt,),
        in_specs=[seq(v0), seq(g0), pl.BlockSpec((CONV_WIDTH, LANES), lambda i: (0, i)), pl.BlockSpec((1, LANES), lambda i: (0, i))],
        out_specs=seq(0), out_shape=jax.ShapeDtypeStruct((S, C), F32),
        scratch_shapes=[pltpu.VMEM((S + CONV_PAD, LANES), F32)],
        compiler_params=_params(("parallel",)),
    )(proj, proj, w, b)


def _conv_bwd_call(proj, col0, w, dc1):
    S = proj.shape[0]
    C = w.shape[1]
    nt = C // LANES
    v0, g0 = col0 // LANES, (col0 + C) // LANES

    def body(val_ref, gate_ref, w_ref, dy_ref, dval_ref, dgate_ref, dw_ref, db_ref, xpad_ref, dpad_ref, dwacc_ref):
        xpad_ref[0:CONV_PAD, :] = jnp.zeros((CONV_PAD, LANES), F32)
        xpad_ref[CONV_PAD:, :] = val_ref[...] * _sigmoid(gate_ref[...])
        dpad_ref[0:S, :] = dy_ref[...]
        dpad_ref[S:, :] = jnp.zeros((CONV_PAD, LANES), F32)
        dwacc_ref[...] = jnp.zeros_like(dwacc_ref)

        def step(t0):
            rows = pl.ds(t0, TIME_BLOCK)
            _conv_weight_grad(xpad_ref[pl.ds(t0, TIME_BLOCK + CONV_PAD), :], dy_ref[rows, :], dwacc_ref, CONV_WIDTH, CONV_PAD)
            dc0 = _conv_transpose_block(dpad_ref[pl.ds(t0, TIME_BLOCK + CONV_PAD), :], w_ref, CONV_WIDTH, CONV_PAD)
            sg = _sigmoid(gate_ref[rows, :])
            dval_ref[rows, :] = (dc0 * sg).astype(BF16)
            dgate_ref[rows, :] = (dc0 * val_ref[rows, :] * sg * (1.0 - sg)).astype(BF16)

        _time_loop(S, step)
        dw_ref[...] = dwacc_ref[...]
        db_ref[...] = jnp.sum(dy_ref[...], axis=0, keepdims=True)

    seq = lambda off: pl.BlockSpec((S, LANES), lambda i: (0, off + i))
    return pl.pallas_call(
        body, name="conv_module_bwd", grid=(nt,),
        in_specs=[seq(v0), seq(g0), pl.BlockSpec((CONV_WIDTH, LANES), lambda i: (0, i)), seq(0)],
        out_specs=[seq(0), seq(0), pl.BlockSpec((CONV_PAD, LANES), lambda i: (0, i)), pl.BlockSpec((1, LANES), lambda i: (0, i))],
        out_shape=[jax.ShapeDtypeStruct((S, C), BF16), jax.ShapeDtypeStruct((S, C), BF16),
                   jax.ShapeDtypeStruct((CONV_PAD, C), F32), jax.ShapeDtypeStruct((1, C), F32)],
        scratch_shapes=[pltpu.VMEM((S + CONV_PAD, LANES), F32), pltpu.VMEM((S + CONV_PAD, LANES), F32),
                        pltpu.VMEM((CONV_PAD, LANES), F32)],
        compiler_params=_params(("parallel",)),
    )(proj, proj, w, dc1)


def _ffn_fwd_call(u, w, b):
    S, C2 = u.shape
    C = C2 // 2
    nt = C // LANES

    def body(ug_ref, uv_ref, wg_ref, wv_ref, bg_ref, bv_ref, f_ref, pg_ref, pv_ref):
        zeros = jnp.zeros((FFN_PAD, LANES), F32)
        pg_ref[0:FFN_PAD, :] = zeros
        pv_ref[0:FFN_PAD, :] = zeros
        pg_ref[FFN_PAD:, :] = ug_ref[...]
        pv_ref[FFN_PAD:, :] = uv_ref[...]

        def step(t0):
            win = pl.ds(t0, TIME_BLOCK + FFN_PAD)
            cg = _conv_block(pg_ref[win, :], wg_ref, FFN_CONV_WIDTH, FFN_PAD) + bg_ref[...]
            cv = _conv_block(pv_ref[win, :], wv_ref, FFN_CONV_WIDTH, FFN_PAD) + bv_ref[...]
            f_ref[pl.ds(t0, TIME_BLOCK), :] = (_gelu(cg) * cv).astype(BF16)

        _time_loop(S, step)

    seq = lambda off: pl.BlockSpec((S, LANES), lambda i: (0, off + i))
    wsp = lambda off: pl.BlockSpec((FFN_CONV_WIDTH, LANES), lambda i: (0, off + i))
    bsp = lambda off: pl.BlockSpec((1, LANES), lambda i: (0, off + i))
    return pl.pallas_call(
        body, name="ffn_conv_geglu", grid=(nt,),
        in_specs=[seq(0), seq(nt), wsp(0), wsp(nt), bsp(0), bsp(nt)],
        out_specs=seq(0), out_shape=jax.ShapeDtypeStruct((S, C), BF16),
        scratch_shapes=[pltpu.VMEM((S + FFN_PAD, LANES), F32)] * 2,
        compiler_params=_params(("parallel",)),
    )(u, u, w, w, b, b)


def _ffn_bwd_call(u, w, b, df):
    S, C2 = u.shape
    C = C2 // 2
    nt = C // LANES

    def body(ug_ref, uv_ref, wg_ref, wv_ref, bg_ref, bv_ref, df_ref,
             dug_ref, duv_ref, dwg_ref, dwv_ref, dbg_ref, dbv_ref,
             pg_ref, pv_ref, dg_ref, dv_ref, dwg_acc, dwv_acc, dbg_acc, dbv_acc):
        zeros = jnp.zeros((FFN_PAD, LANES), F32)
        pg_ref[0:FFN_PAD, :] = zeros
        pv_ref[0:FFN_PAD, :] = zeros
        pg_ref[FFN_PAD:, :] = ug_ref[...]
        pv_ref[FFN_PAD:, :] = uv_ref[...]
        dg_ref[S:, :] = zeros
        dv_ref[S:, :] = zeros
        dwg_acc[...] = jnp.zeros_like(dwg_acc)
        dwv_acc[...] = jnp.zeros_like(dwv_acc)
        dbg_acc[...] = jnp.zeros_like(dbg_acc)
        dbv_acc[...] = jnp.zeros_like(dbv_acc)

        def first(t0):
            win = pl.ds(t0, TIME_BLOCK + FFN_PAD)
            rows = pl.ds(t0, TIME_BLOCK)
            xg = pg_ref[win, :]
            xv = pv_ref[win, :]
            cg = _conv_block(xg, wg_ref, FFN_CONV_WIDTH, FFN_PAD) + bg_ref[...]
            cv = _conv_block(xv, wv_ref, FFN_CONV_WIDTH, FFN_PAD) + bv_ref[...]
            dfb = df_ref[rows, :]
            dcg = dfb * cv * _gelu_grad(cg)
            dcv = dfb * _gelu(cg)
            dg_ref[rows, :] = dcg
            dv_ref[rows, :] = dcv
            _conv_weight_grad(xg, dcg, dwg_acc, FFN_CONV_WIDTH, FFN_PAD)
            _conv_weight_grad(xv, dcv, dwv_acc, FFN_CONV_WIDTH, FFN_PAD)
            dbg_acc[...] += jnp.sum(dcg, axis=0, keepdims=True)
            dbv_acc[...] += jnp.sum(dcv, axis=0, keepdims=True)

        def second(t0):
            win = pl.ds(t0, TIME_BLOCK + FFN_PAD)
            rows = pl.ds(t0, TIME_BLOCK)
            dug_ref[rows, :] = _conv_transpose_block(dg_ref[win, :], wg_ref, FFN_CONV_WIDTH, FFN_PAD).astype(BF16)
            duv_ref[rows, :] = _conv_transpose_block(dv_ref[win, :], wv_ref, FFN_CONV_WIDTH, FFN_PAD).astype(BF16)

        _time_loop(S, first)
        _time_loop(S, second)
        dwg_ref[...] = dwg_acc[...]
        dwv_ref[...] = dwv_acc[...]
        dbg_ref[...] = dbg_acc[...]
        dbv_ref[...] = dbv_acc[...]

    seq = lambda off: pl.BlockSpec((S, LANES), lambda i: (0, off + i))
    wsp = lambda off: pl.BlockSpec((FFN_CONV_WIDTH, LANES), lambda i: (0, off + i))
    bsp = lambda off: pl.BlockSpec((1, LANES), lambda i: (0, off + i))
    return pl.pallas_call(
        body, name="ffn_conv_geglu_bwd", grid=(nt,),
        in_specs=[seq(0), seq(nt), wsp(0), wsp(nt), bsp(0), bsp(nt), seq(0)],
        out_specs=[seq(0), seq(0), pl.BlockSpec((SUBLANES, LANES), lambda i: (0, i)), pl.BlockSpec((SUBLANES, LANES), lambda i: (0, i)),
                   bsp(0), bsp(0)],
        out_shape=[jax.ShapeDtypeStruct((S, C), BF16)] * 2 + [jax.ShapeDtypeStruct((SUBLANES, C), F32)] * 2
        + [jax.ShapeDtypeStruct((1, C), F32)] * 2,
        scratch_shapes=[pltpu.VMEM((S + FFN_PAD, LANES), F32)] * 4 + [pltpu.VMEM((SUBLANES, LANES), F32)] * 2
        + [pltpu.VMEM((1, LANES), F32)] * 2,
        compiler_params=_params(("parallel",)),
    )(u, u, w, w, b, b, df)


def _adamw_call(w, g, m, v, name):
    R, C = w.shape
    tr = _row_tile(R, C)
    c1 = 1.0 / (1.0 - ADAM_B1 ** ADAM_STEP)
    c2 = 1.0 / (1.0 - ADAM_B2 ** ADAM_STEP)

    def body(w_ref, g_ref, m_ref, v_ref, d_ref, mo_ref, vo_ref):
        gv = g_ref[...]
        mn = ADAM_B1 * m_ref[...] + (1.0 - ADAM_B1) * gv
        vn = ADAM_B2 * v_ref[...] + (1.0 - ADAM_B2) * (gv * gv)
        mo_ref[...] = mn
        vo_ref[...] = vn
        d_ref[...] = -ADAM_LR * ((mn * c1) / (jnp.sqrt(vn * c2) + ADAM_EPS) + ADAM_WD * w_ref[...])

    spec = pl.BlockSpec((tr, C), lambda i: (i, 0))
    return pl.pallas_call(
        body, name=name, grid=(R // tr,),
        in_specs=[spec] * 4, out_specs=[spec] * 3,
        out_shape=[jax.ShapeDtypeStruct((R, C), F32)] * 3,
        compiler_params=_params(("parallel",)),
    )(w, g, m, v)


def _position():
    return lax.axis_index("x"), lax.axis_index("y"), lax.axis_index("c")


def _chip_peers(x, y):
    return [(x, 1 - y), (1 - x, y), (1 - x, 1 - y)]


def _half_rows(ref, core, rows):
    h = rows // 2
    start = pl.multiple_of(core * h, 16)
    return ref.at[pl.ds(start, h), :] if len(ref.shape) == 2 else ref.at[:, pl.ds(start, h), :]


def _shard_half(ref, shard, core, rows):
    h = rows // 2
    return ref.at[shard, pl.ds(pl.multiple_of(core * h, 16), h), :]


ANY = pl.BlockSpec(memory_space=pl.ANY)


def _allgather_call(shards, whole):
    n, nw = len(shards), len(whole)
    outs_shape = [jax.ShapeDtypeStruct((N_CHIPS,) + s.shape, s.dtype) for s in shards + whole]

    def body(*refs):
        ins, outs = refs[:n + nw], refs[n + nw:2 * (n + nw)]
        send_sems, recv_sems, pass_send, pass_recv, own_send, own_recv = refs[2 * (n + nw):]
        x, y, c = _position()
        chip = 2 * x + y
        peers = _chip_peers(x, y)
        sent, local = [], []
        for i in range(n + nw):
            cp = pltpu.make_async_remote_copy(src_ref=ins[i], dst_ref=outs[i].at[chip], send_sem=own_send.at[i],
                                              recv_sem=own_recv.at[i], device_id=(x, y, 1 - c), device_id_type=MESH)
            cp.start()
            local.append(cp)
            rows = ins[i].shape[0]
            for k, (px, py) in enumerate(peers):
                if i < n:
                    src, dst = _half_rows(ins[i], c, rows), _shard_half(outs[i], chip, c, rows)
                else:
                    src, dst = ins[i], outs[i].at[chip]
                cp = pltpu.make_async_remote_copy(src_ref=src, dst_ref=dst, send_sem=send_sems.at[i, k],
                                                  recv_sem=recv_sems.at[i, k], device_id=(px, py, c), device_id_type=MESH)
                cp.start()
                sent.append(cp)
        passed = []
        for i in range(n + nw):
            rows = ins[i].shape[0]
            for k, (px, py) in enumerate(peers):
                landed = _shard_half(outs[i], 2 * px + py, c, rows) if i < n else outs[i].at[2 * px + py]
                pltpu.make_async_remote_copy(src_ref=landed, dst_ref=landed, send_sem=send_sems.at[i, k],
                                             recv_sem=recv_sems.at[i, k], device_id=(px, py, c), device_id_type=MESH).wait_recv()
                if i < n:
                    cp = pltpu.make_async_remote_copy(src_ref=landed, dst_ref=landed, send_sem=pass_send.at[i, k],
                                                      recv_sem=pass_recv.at[i, k], device_id=(x, y, 1 - c), device_id_type=MESH)
                    cp.start()
                    passed.append(cp)
        for cp in sent:
            cp.wait_send()
        for cp in passed:
            cp.wait()
        for cp in local:
            cp.wait()

    return pl.pallas_call(
        body, name="weight_allgather",
        in_specs=[ANY] * (n + nw), out_specs=[ANY] * (n + nw), out_shape=outs_shape,
        scratch_shapes=[pltpu.SemaphoreType.DMA((n + nw, 3)), pltpu.SemaphoreType.DMA((n + nw, 3)),
                        pltpu.SemaphoreType.DMA((n, 3)), pltpu.SemaphoreType.DMA((n, 3)),
                        pltpu.SemaphoreType.DMA((n + nw,)), pltpu.SemaphoreType.DMA((n + nw,))],
    )(*shards, *whole)


HBM_SPEC = pl.BlockSpec(memory_space=pltpu.HBM)
SEM_SPEC = pl.BlockSpec(memory_space=pltpu.SEMAPHORE)
DATAFLOW = pltpu.SideEffectType.DATAFLOW_SIDE_EFFECTING


def _in_hbm(a):
    return pltpu.with_memory_space_constraint(a, pltpu.HBM)


def _split_start(name, srcs, lands, n_sems, copies, after):
    n, m = len(srcs), len(lands)

    def body(*refs):
        src_refs, land_refs = refs[:n], refs[n:n + m]
        send_sem, recv_sem = refs[n + m + 1], refs[n + m + 2]
        token = refs[-1]
        for src, dst, dev, idx in copies(src_refs, land_refs):
            pltpu.make_async_remote_copy(src_ref=src, dst_ref=dst, send_sem=send_sem.at[idx], recv_sem=recv_sem.at[idx],
                                         device_id=dev, device_id_type=MESH).start()
        token[...] = jnp.zeros_like(token)

    outs = pl.pallas_call(
        body, name=name,
        in_specs=[HBM_SPEC] * (n + m) + [ANY],
        out_specs=[SEM_SPEC, SEM_SPEC] + [HBM_SPEC] * (n + m) + [pl.BlockSpec(memory_space=pltpu.VMEM)],
        out_shape=[pltpu.SemaphoreType.DMA((n_sems,)), pltpu.SemaphoreType.DMA((n_sems,))]
        + [pltpu.HBM(a.shape, a.dtype) for a in list(srcs) + list(lands)] + [jax.ShapeDtypeStruct((SUBLANES, LANES), F32)],
        input_output_aliases={i: 2 + i for i in range(n + m)},
        compiler_params=pltpu.CompilerParams(has_side_effects=DATAFLOW),
    )(*[_in_hbm(a) for a in list(srcs) + list(lands)], after)
    return dict(send=outs[0], recv=outs[1], srcs=list(outs[2:2 + n]), lands=list(outs[2 + n:2 + n + m]),
                tile=outs[-1], token=outs[-1][0, 0])


def _split_wait(name, started, copies, after):
    n, m = len(started["srcs"]), len(started["lands"])

    def body(*refs):
        src_refs, land_refs = refs[:n], refs[n:n + m]
        send_sem, recv_sem = refs[n + m], refs[n + m + 1]
        for src, dst, dev, idx in copies(src_refs, land_refs):
            cp = pltpu.make_async_remote_copy(src_ref=src, dst_ref=dst, send_sem=send_sem.at[idx], recv_sem=recv_sem.at[idx],
                                              device_id=dev, device_id_type=MESH)
            cp.wait_send()
            cp.wait_recv()

    arrays = started["srcs"] + started["lands"]
    outs = pl.pallas_call(
        body, name=name,
        in_specs=[HBM_SPEC] * (n + m) + [SEM_SPEC, SEM_SPEC, ANY],
        out_specs=[HBM_SPEC] * (n + m),
        out_shape=[pltpu.HBM(a.shape, a.dtype) for a in arrays],
        input_output_aliases={i: i for i in range(n + m)},
        compiler_params=pltpu.CompilerParams(has_side_effects=DATAFLOW),
    )(*arrays, started["send"], started["recv"], after)
    return list(outs[n:])


def _gather_copies(srcs, lands):
    x, y, c = _position()
    chip = 2 * x + y
    targets = [(px, py, c) for px, py in _chip_peers(x, y)] + [(x, y, 1 - c)]
    return [(s, l.at[chip], dev, len(targets) * i + k) for i, (s, l) in enumerate(zip(srcs, lands)) for k, dev in enumerate(targets)]


def _exchange_copies(srcs, lands):
    x, y, c = _position()
    return [(srcs[0].at[2 * px + py], lands[0].at[k], (px, py, c), k) for k, (px, py) in enumerate(_chip_peers(x, y))]


def _sibling_exchange_call(grads, name="grad_sibling_exchange"):
    n = len(grads)
    halves = [jax.ShapeDtypeStruct((N_CHIPS, g.shape[1] // 2, g.shape[2]), F32) for g in grads]

    def body(*refs):
        ins, outs = refs[:n], refs[n:2 * n]
        send_sems, recv_sems = refs[2 * n:]
        x, y, c = _position()
        copies = []
        for i in range(n):
            cp = pltpu.make_async_remote_copy(src_ref=_half_rows(ins[i], 1 - c, grads[i].shape[1]), dst_ref=outs[i],
                                              send_sem=send_sems.at[i], recv_sem=recv_sems.at[i],
                                              device_id=(x, y, 1 - c), device_id_type=MESH)
            cp.start()
            copies.append(cp)
        for cp in copies:
            cp.wait()

    return pl.pallas_call(
        body, name=name,
        in_specs=[ANY] * n, out_specs=[ANY] * n, out_shape=halves,
        scratch_shapes=[pltpu.SemaphoreType.DMA((n,)), pltpu.SemaphoreType.DMA((n,))],
    )(*grads)


def _pair_sum_call(grad, recv, core, name):
    _, h, B = recv.shape
    tr = _row_tile(h, B)

    def body(core_ref, g_ref, r_ref, o_ref, ob_ref):
        s = g_ref[...] + r_ref[...]
        o_ref[...] = s
        ob_ref[...] = s.astype(BF16)

    g_spec = pl.BlockSpec((None, tr, B), lambda q, i, core_ref: (q, core_ref[0] * (h // tr) + i, 0))
    spec = pl.BlockSpec((None, tr, B), lambda q, i, core_ref: (q, i, 0))
    return pl.pallas_call(
        body, name=name,
        grid_spec=pltpu.PrefetchScalarGridSpec(num_scalar_prefetch=1, grid=(N_CHIPS, h // tr), in_specs=[g_spec, spec],
                                               out_specs=[spec, spec]),
        out_shape=[jax.ShapeDtypeStruct(recv.shape, F32), jax.ShapeDtypeStruct(recv.shape, BF16)],
        compiler_params=_params(("parallel", "parallel")),
    )(core, grad, recv)


def _chip_exchange_call(partials):
    n = len(partials)
    outs_shape = [jax.ShapeDtypeStruct((3,) + t.shape[1:], BF16) for t in partials]

    def body(*refs):
        ins, outs = refs[:n], refs[n:2 * n]
        send_sems, recv_sems = refs[2 * n:]
        x, y, c = _position()
        copies = []
        for i in range(n):
            for k, (px, py) in enumerate(_chip_peers(x, y)):
                cp = pltpu.make_async_remote_copy(src_ref=ins[i].at[2 * px + py], dst_ref=outs[i].at[k], send_sem=send_sems.at[i, k],
                                                  recv_sem=recv_sems.at[i, k], device_id=(px, py, c), device_id_type=MESH)
                cp.start()
                copies.append(cp)
        for cp in copies:
            cp.wait()

    return pl.pallas_call(
        body, name="grad_chip_exchange",
        in_specs=[ANY] * n, out_specs=[ANY] * n, out_shape=outs_shape,
        scratch_shapes=[pltpu.SemaphoreType.DMA((n, 3)), pltpu.SemaphoreType.DMA((n, 3))],
    )(*partials)


def _chip_sum_call(partial, recv, chip_core, name):
    _, h, B = recv.shape
    tr = _row_tile(h, B)

    def body(cc_ref, p_ref, r_ref, o_ref):
        o_ref[...] = ((p_ref[...] + r_ref[0].astype(F32)) + r_ref[1].astype(F32)) + r_ref[2].astype(F32)

    return pl.pallas_call(
        body, name=name,
        grid_spec=pltpu.PrefetchScalarGridSpec(
            num_scalar_prefetch=1, grid=(h // tr,),
            in_specs=[pl.BlockSpec((None, tr, B), lambda i, cc_ref: (cc_ref[0], i, 0)),
                      pl.BlockSpec((3, tr, B), lambda i, cc_ref: (0, i, 0))],
            out_specs=pl.BlockSpec((tr, B), lambda i, cc_ref: (cc_ref[1] * (h // tr) + i, 0))),
        out_shape=jax.ShapeDtypeStruct((2 * h, B), F32),
        compiler_params=_params(("parallel",)),
    )(chip_core, partial, recv)


def _sibling_assemble_call(shards, name="grad_sibling_assemble"):
    n = len(shards)

    def body(*refs):
        ins, outs = refs[:n], refs[n:2 * n]
        send_sems, recv_sems = refs[2 * n:]
        x, y, c = _position()
        copies = []
        for i in range(n):
            rows = shards[i].shape[0]
            cp = pltpu.make_async_remote_copy(src_ref=_half_rows(ins[i], c, rows), dst_ref=_half_rows(outs[i], c, rows),
                                              send_sem=send_sems.at[i], recv_sem=recv_sems.at[i],
                                              device_id=(x, y, 1 - c), device_id_type=MESH)
            cp.start()
            copies.append(cp)
        for cp in copies:
            cp.wait()

    return pl.pallas_call(
        body, name=name,
        in_specs=[ANY] * n, out_specs=[ANY] * n,
        out_shape=[jax.ShapeDtypeStruct(s.shape, F32) for s in shards],
        input_output_aliases={i: i for i in range(n)},
        scratch_shapes=[pltpu.SemaphoreType.DMA((n,)), pltpu.SemaphoreType.DMA((n,))],
    )(*shards)


def _small_allreduce_call(packed):
    rows = packed.shape[0]

    def body(x_ref, o_ref, buf_ref, send_sems, recv_sems):
        x, y, c = _position()
        me = 4 * x + 2 * y + c
        buf_ref[me] = x_ref[...]
        copies = []
        for k in range(1, 8):
            peer = (1 - x if k & 4 else x, 1 - y if k & 2 else y, 1 - c if k & 1 else c)
            cp = pltpu.make_async_remote_copy(src_ref=buf_ref.at[me], dst_ref=buf_ref.at[me], send_sem=send_sems.at[k - 1],
                                              recv_sem=recv_sems.at[k - 1], device_id=peer, device_id_type=MESH)
            cp.start()
            copies.append(cp)
        for cp in copies:
            cp.wait()
        acc = buf_ref[0]
        for d in range(1, 8):
            acc = acc + buf_ref[d]
        o_ref[...] = acc

    return pl.pallas_call(
        body, name="small_grad_allreduce",
        in_specs=[pl.BlockSpec(memory_space=pltpu.VMEM)], out_specs=pl.BlockSpec(memory_space=pltpu.VMEM),
        out_shape=jax.ShapeDtypeStruct((rows, LANES), F32),
        scratch_shapes=[pltpu.VMEM((8, rows, LANES), F32), pltpu.SemaphoreType.DMA((7,)), pltpu.SemaphoreType.DMA((7,))],
    )(packed)


def _pack(arrays):
    flat = jnp.concatenate([a.reshape(-1).astype(F32) for a in arrays])
    rows = -(-flat.shape[0] // LANES)
    rows = -(-rows // SUBLANES) * SUBLANES
    flat = jnp.pad(flat, (0, rows * LANES - flat.shape[0]))
    return flat.reshape(rows, LANES)


def _unpack(packed, shapes):
    flat = packed.reshape(-1)
    out, off = [], 0
    for shp in shapes:
        size = int(np.prod(shp))
        out.append(flat[off:off + size].reshape(shp))
        off += size
    return out


def _local_step(xs, target, P, late_weights, on_grad):
    S, D = xs.shape
    qkv_width = 3 * N_HEADS * HEAD_DIM
    glu_col0, gate_col0 = qkv_width, qkv_width + 2 * D
    shard_major = lambda g: g.reshape(N_CHIPS, g.shape[0] // N_CHIPS, g.shape[1])

    h1 = _rms_fwd_call(xs, P["norm_mix_pre"])
    proj = _matmul(h1, P["w_in"], "nn", "proj_in")
    buckets = _bucket_tables()
    bias = _bias_table_call(P["rel_bias"], buckets)
    qkv, parts = [], []
    for g in range(N_GROUPS):
        q, k, v = [_to_group_order(proj, 3 * t + g, g, f"group_order_{'qkv'[t]}{g}") for t in range(3)]
        qkv.append((q, k, v))
        o_g, lse_g = _attn_fwd_call(q, k, v, bias, g)
        parts += [_to_token_order(o_g, g, f"token_order_o{g}"), _to_token_order(lse_g, g, f"token_order_lse{g}")]
    a, a_bf, lse = _attn_merge_call(parts)
    P = dict(P, **late_weights("mix", a_bf))
    y_a = _matmul(a_bf, P["w_attn_out"], "nn", "attn_out")
    c1 = _conv_fwd_call(proj, glu_col0, P["conv_dw_w"], P["conv_dw_b"])
    cact = _ln_silu_call(c1, P["conv_ln_g"], P["conv_ln_b"])
    y_c = _matmul(cact, P["conv_pw_w"], "nn", "conv_pw")
    mixed = _mix_call(proj, gate_col0, P["b_gate"], y_a, y_c)
    out = _matmul(mixed, P["w_out"], "nn", "mix_out")
    x1, h2 = _res1_call(xs, out, P["norm_mix_post"], P["norm_ffn_pre"])
    P = dict(P, **late_weights("ffn", h2))
    u = _matmul(h2, P["w_up"], "nn", "ffn_up")
    f = _ffn_fwd_call(u, P["ffn_conv_w"], P["ffn_conv_b"])
    yff = _matmul(f, P["w_down"], "nn", "ffn_down")
    loss_tile, dx2 = _loss_call(yff, x1, P["norm_ffn_post"], target)

    G = {}
    dyff, G["norm_ffn_post"] = _rms_bwd_call(yff, P["norm_ffn_post"], dx2, "rms_ffn_post_bwd")
    zero = on_grad("w_down", shard_major(_matmul(f, dyff, "tn", "ffn_down_dw")))
    df = _matmul(dyff, P["w_down"], "nt", "ffn_down_dx")
    dug, duv, dwg, dwv, dbg, dbv = _ffn_bwd_call(u, P["ffn_conv_w"], P["ffn_conv_b"] + zero, df)
    G["ffn_conv_w"] = jnp.concatenate([dwg[:FFN_CONV_WIDTH], dwv[:FFN_CONV_WIDTH]], axis=1)
    G["ffn_conv_b"] = jnp.concatenate([dbg, dbv], axis=1)
    du = jnp.concatenate([dug, duv], axis=1)
    zero = on_grad("w_up", _matmul(h2, du, "tn", "ffn_up_dw", out_shards=True))
    dh2 = _matmul(du, P["w_up"], "nt", "ffn_up_dx")
    dx1, dout, G["norm_ffn_pre"], G["norm_mix_post"] = _mid_bwd_call(x1, P["norm_ffn_pre"] + zero, dh2, dx2, out, P["norm_mix_post"])
    zero = on_grad("w_out", shard_major(_matmul(mixed, dout, "tn", "mix_out_dw")))
    dmixed = _matmul(dout, P["w_out"], "nt", "mix_out_dx")
    dya, dyc, dga, dgc, dba, dbc = _mix_bwd_call(dmixed, proj, gate_col0, P["b_gate"] + zero, y_a, y_c)
    G["b_gate"] = jnp.concatenate([dba, dbc], axis=1)
    zero = on_grad("w_attn_out", _matmul(a_bf, dya, "tn", "attn_out_dw", out_shards=True))
    zero = zero + on_grad("conv_pw_w", shard_major(_matmul(cact, dyc, "tn", "conv_pw_dw")))
    da = _matmul(dya, P["w_attn_out"], "nt", "attn_out_dx")
    dcact = _matmul(dyc, P["conv_pw_w"], "nt", "conv_pw_dx")
    dc1, G["conv_ln_g"], G["conv_ln_b"] = _ln_silu_bwd_call(c1, P["conv_ln_g"] + zero, P["conv_ln_b"], dcact)
    dval, dgate, dw_dw, G["conv_dw_b"] = _conv_bwd_call(proj, glu_col0, P["conv_dw_w"], dc1)
    G["conv_dw_w"] = dw_dw[:CONV_WIDTH]
    delta = _attn_delta_call(a, da)
    dqs, dks, dvs, dbs = [], [], [], []
    for g in range(N_GROUPS):
        grouped = [_to_group_order(t, 0, g, f"group_order_{n}{g}")[0] for t, n in ((da, "da"), (lse, "lse"), (delta, "delta"))]
        dq, dk, dv, db = _attn_bwd_call(*qkv[g], bias, *grouped, g)
        dqs.append(_to_token_order(dq, g, f"token_order_dq{g}"))
        dks.append(_to_token_order(dk, g, f"token_order_dk{g}"))
        dvs.append(_to_token_order(dv, g, f"token_order_dv{g}"))
        dbs.append(db)
    G["rel_bias"] = _bias_grad_call(jnp.concatenate(dbs, axis=0), buckets)
    dproj = jnp.concatenate([t.astype(BF16) for t in dqs + dks + dvs] + [dval, dgate, dga, dgc], axis=1)
    zero = on_grad("w_in", _matmul(h1, dproj, "tn", "proj_in_dw", out_shards=True, tm=512))
    dh1 = _matmul(dproj, P["w_in"], "nt", "proj_in_dx")
    grad_x, G["norm_mix_pre"] = _in_bwd_call(xs, P["norm_mix_pre"] + zero, dh1, dx1)
    return loss_tile, grad_x, G


def kernel(x, w_in, b_gate, rel_bias, w_attn_out, conv_dw_w, conv_dw_b, conv_ln_g, conv_ln_b, conv_pw_w, w_out, norm_mix_pre, norm_mix_post, norm_ffn_pre, norm_ffn_post, w_up, ffn_conv_w, ffn_conv_b, w_down, loss_target, m_w_in, m_b_gate, m_rel_bias, m_w_attn_out, m_conv_dw_w, m_conv_dw_b, m_conv_ln_g, m_conv_ln_b, m_conv_pw_w, m_w_out, m_norm_mix_pre, m_norm_mix_post, m_norm_ffn_pre, m_norm_ffn_post, m_w_up, m_ffn_conv_w, m_ffn_conv_b, m_w_down, v_w_in, v_b_gate, v_rel_bias, v_w_attn_out, v_conv_dw_w, v_conv_dw_b, v_conv_ln_g, v_conv_ln_b, v_conv_pw_w, v_w_out, v_norm_mix_pre, v_norm_mix_post, v_norm_ffn_pre, v_norm_ffn_post, v_w_up, v_ffn_conv_w, v_ffn_conv_b, v_w_down):
    weights = dict(w_in=w_in, b_gate=b_gate, rel_bias=rel_bias, w_attn_out=w_attn_out, conv_dw_w=conv_dw_w, conv_dw_b=conv_dw_b,
                   conv_ln_g=conv_ln_g, conv_ln_b=conv_ln_b, conv_pw_w=conv_pw_w, w_out=w_out, norm_mix_pre=norm_mix_pre,
                   norm_mix_post=norm_mix_post, norm_ffn_pre=norm_ffn_pre, norm_ffn_post=norm_ffn_post, w_up=w_up,
                   ffn_conv_w=ffn_conv_w, ffn_conv_b=ffn_conv_b, w_down=w_down)
    m_in = dict(w_in=m_w_in, b_gate=m_b_gate, rel_bias=m_rel_bias, w_attn_out=m_w_attn_out, conv_dw_w=m_conv_dw_w,
                conv_dw_b=m_conv_dw_b, conv_ln_g=m_conv_ln_g, conv_ln_b=m_conv_ln_b, conv_pw_w=m_conv_pw_w, w_out=m_w_out,
                norm_mix_pre=m_norm_mix_pre, norm_mix_post=m_norm_mix_post, norm_ffn_pre=m_norm_ffn_pre,
                norm_ffn_post=m_norm_ffn_post, w_up=m_w_up, ffn_conv_w=m_ffn_conv_w, ffn_conv_b=m_ffn_conv_b, w_down=m_w_down)
    v_in = dict(w_in=v_w_in, b_gate=v_b_gate, rel_bias=v_rel_bias, w_attn_out=v_w_attn_out, conv_dw_w=v_conv_dw_w,
                conv_dw_b=v_conv_dw_b, conv_ln_g=v_conv_ln_g, conv_ln_b=v_conv_ln_b, conv_pw_w=v_conv_pw_w, w_out=v_w_out,
                norm_mix_pre=v_norm_mix_pre, norm_mix_post=v_norm_mix_post, norm_ffn_pre=v_norm_ffn_pre,
                norm_ffn_post=v_norm_ffn_post, w_up=v_w_up, ffn_conv_w=v_ffn_conv_w, ffn_conv_b=v_ffn_conv_b, w_down=v_w_down)
    names = list(weights)
    xi, yi, ci = _position()
    chip = 2 * xi + yi
    core_arr = jnp.reshape(ci, (1,)).astype(jnp.int32)

    xs = x[0]
    target = loss_target[0]
    S, D = xs.shape

    big = ["w_in", "w_attn_out", "conv_pw_w", "w_out", "w_up", "w_down"]
    row_sharded = ("conv_pw_w", "w_out", "w_down")
    bf16_shard = {k: weights[k][0].astype(BF16) for k in big}
    natural = lambda k, g: g.reshape(-1, g.shape[2]) if k in row_sharded else g
    w_in_full, dw4, fc4 = _allgather_call([bf16_shard["w_in"]], [conv_dw_w[0], ffn_conv_w[0]])
    late_sets = dict(mix=["w_attn_out", "conv_pw_w", "w_out"], ffn=["w_up", "w_down"])
    started, after = {}, w_in_full
    for tag, keys in late_sets.items():
        srcs = [bf16_shard[k] for k in keys]
        lands = [lax.empty((N_CHIPS,) + s.shape, BF16) for s in srcs]
        started[tag] = _split_start(f"gather_{tag}_start", srcs, lands, 4 * len(keys), _gather_copies, after)
        after = started[tag]["tile"]
    launched = started["mix"]["token"] + started["ffn"]["token"]

    def late_weights(tag, after):
        landed = _split_wait(f"gather_{tag}_wait", started[tag], _gather_copies, after)
        return {k: natural(k, g) for k, g in zip(late_sets[tag], landed)}

    chip_core = jnp.stack([chip, ci]).astype(jnp.int32)
    pending = {}

    def on_grad(k, g3):
        (r1,) = _sibling_exchange_call([g3], f"grad_sibling_exchange_{k}")
        s32, s16 = _pair_sum_call(g3, r1, core_arr, f"pair_sum_{k}")
        land = lax.empty((3,) + s16.shape[1:], BF16)
        pending[k] = (s32, _split_start(f"chip_exchange_start_{k}", [s16], [land], 3, _exchange_copies, s32))
        return pending[k][1]["token"]

    def finish(keys, after, tag):
        halves = []
        for k in keys:
            s32, st = pending[k]
            (recv2,) = _split_wait(f"chip_exchange_wait_{k}", st, _exchange_copies, after)
            halves.append(_chip_sum_call(s32, recv2, chip_core, f"chip_sum_{k}"))
        return dict(zip(keys, _sibling_assemble_call(halves, f"grad_sibling_assemble_{tag}")))

    P = dict(w_in=w_in_full, conv_dw_w=jnp.concatenate(list(dw4), axis=1), ffn_conv_w=jnp.concatenate(list(fc4), axis=1),
             b_gate=b_gate, rel_bias=rel_bias, conv_dw_b=conv_dw_b, conv_ln_g=conv_ln_g, conv_ln_b=conv_ln_b,
             norm_mix_pre=norm_mix_pre + launched, norm_mix_post=norm_mix_post, norm_ffn_pre=norm_ffn_pre,
             norm_ffn_post=norm_ffn_post, ffn_conv_b=ffn_conv_b)
    loss_tile, grad_x, G = _local_step(xs, target, P, late_weights, on_grad)
    loss = lax.psum(loss_tile[0, 0], ("x", "y", "c"))

    small = [k for k in names if k not in big]
    packed = _pack([G[k] for k in small])
    summed_block = _small_allreduce_call(packed)
    summed = _unpack(summed_block, [G[k].shape for k in small])
    reduced = {}
    for k, gsum in zip(small, summed):
        if k in ("conv_dw_w", "ffn_conv_w"):
            cols = weights[k].shape[2]
            reduced[k] = lax.dynamic_slice_in_dim(gsum, chip * cols, cols, axis=1)
        else:
            reduced[k] = gsum

    grads, deltas, new_m, new_v = {}, {}, {}, {}

    def update(keys):
        for k in keys:
            d, mn, vn = _adamw_call(weights[k][0], reduced[k], m_in[k][0], v_in[k][0], f"adamw_{k}")
            grads[k], deltas[k], new_m[k], new_v[k] = reduced[k][None], d[None], mn[None], vn[None]

    others = [k for k in big if k != "w_in"]
    reduced.update(finish(others, summed_block, "others"))
    update(others)
    reduced.update(finish(["w_in"], deltas["w_up"], "w_in"))
    update(["w_in"])
    flat2 = lambda t: t.reshape(-1, t.shape[-1]) if t.ndim == 3 else t
    pw = _pack([flat2(weights[k]) for k in small])
    pg = _pack([reduced[k] for k in small])
    pm = _pack([flat2(m_in[k]) for k in small])
    pv = _pack([flat2(v_in[k]) for k in small])
    pd, pmn, pvn = _adamw_call(pw, pg, pm, pv, "adamw_small")
    shapes = [weights[k].shape for k in small]
    for k, gk, dk_, mk, vk in zip(small, [reduced[k] for k in small], _unpack(pd, shapes), _unpack(pmn, shapes), _unpack(pvn, shapes)):
        grads[k], deltas[k], new_m[k], new_v[k] = gk.reshape(weights[k].shape), dk_, mk, vk

    return (loss, grad_x[None], *[grads[k] for k in names], *[deltas[k] for k in names],
            *[new_m[k] for k in names], *[new_v[k] for k in names])
```

```python
import functools
import math

import jax
import jax.numpy as jnp
import numpy as np
from jax import lax
from jax.experimental import pallas as pl
from jax.experimental.pallas import tpu as pltpu

F32 = jnp.float32
BF16 = jnp.bfloat16
MESH = pl.DeviceIdType.MESH

HEAD_DIM = 128
HEADS_PER_GROUP = 4
DILATED_PATTERNS = ((128, 1), (512, 4), (2048, 16))
N_GROUPS = 3
N_HEADS = N_GROUPS * HEADS_PER_GROUP
SPAN = 128
GROUP_WIDTH = HEADS_PER_GROUP * HEAD_DIM
CONV_WIDTH = 31
FFN_CONV_WIDTH = 3
N_BUCKETS = 32
MAX_DISTANCE = 2048
RMS_EPS = 1e-6
LN_EPS = 1e-5
NEG_INF = -1e30
ADAM_LR = 0.001
ADAM_B1 = 0.9
ADAM_B2 = 0.999
ADAM_EPS = 1e-08
ADAM_WD = 0.01
ADAM_STEP = 10

LANES = 128
SUBLANES = 8
ROW_TILE = 256
TIME_BLOCK = 128
CONV_PAD = 32
FFN_PAD = 8
VMEM_LIMIT = 56 << 20


def _params(sem=None, vmem=None):
    kw = {}
    if sem is not None:
        kw["dimension_semantics"] = sem
    if vmem is not None:
        kw["vmem_limit_bytes"] = vmem
    return pltpu.CompilerParams(**kw)


def _pick(n, cands):
    for c in cands:
        if n % c == 0:
            return c
    return n


ELEMENTWISE_TILE_BYTES = 1 << 20


def _row_tile(rows, cols):
    for c in (512, 256, 128, 64, 32, 16):
        if rows % c == 0 and c * cols * 4 <= ELEMENTWISE_TILE_BYTES:
            return c
    return 16 if rows % 16 == 0 else 8


N_CHIPS = 4
M_TILES = (1024, 1408, 512, 256, 128)
N_TILES = (512, 1408, 256, 128)
K_TILES = (2176, 2048, 1408, 1024, 512, 256, 128)


def _matmul(a, b, mode, name, out_shards=False, tm=None):
    assert a.dtype == BF16 and b.dtype == BF16, (name, a.dtype, b.dtype)
    b3 = b.ndim == 3
    tn = tk = None
    if mode == "nn":
        M, K = a.shape
        N = b.shape[-1] * (N_CHIPS if b3 else 1)
        tn = b.shape[-1] if b3 else None
    elif mode == "nt":
        M, K = a.shape
        N = b.shape[-2]
        tk = b.shape[-1] if b3 else None
    else:
        K, M = a.shape
        N = b.shape[1]
        tn = N // N_CHIPS if out_shards else None
    tm = tm or _pick(M, M_TILES)
    tn = tn or _pick(N, N_TILES)
    tk = tk or _pick(K, K_TILES)
    nk = K // tk
    dn = {"nn": (((1,), (0,)), ((), ())), "nt": (((1,), (1,)), ((), ())), "tn": (((0,), (0,)), ((), ()))}[mode]

    def body(a_ref, b_ref, o_ref):
        if nk == 1:
            o_ref[...] = lax.dot_general(a_ref[...], b_ref[...], dn, preferred_element_type=F32)
        else:
            @pl.when(pl.program_id(2) == 0)
            def _():
                o_ref[...] = jnp.zeros_like(o_ref)

            o_ref[...] += lax.dot_general(a_ref[...], b_ref[...], dn, preferred_element_type=F32)

    if mode == "tn":
        a_spec = pl.BlockSpec((tk, tm), lambda i, j, k: (k, i))
    else:
        a_spec = pl.BlockSpec((tm, tk), lambda i, j, k: (i, k))
    if mode == "nn":
        b_spec = pl.BlockSpec((None, tk, tn), lambda i, j, k: (j, k, 0)) if b3 else pl.BlockSpec((tk, tn), lambda i, j, k: (k, j))
    elif mode == "nt":
        b_spec = pl.BlockSpec((None, tn, tk), lambda i, j, k: (k, j, 0)) if b3 else pl.BlockSpec((tn, tk), lambda i, j, k: (j, k))
    else:
        b_spec = pl.BlockSpec((tk, tn), lambda i, j, k: (k, j))
    if out_shards:
        out_spec = pl.BlockSpec((None, tm, tn), lambda i, j, k: (j, i, 0))
        out_shape = jax.ShapeDtypeStruct((N_CHIPS, M, tn), F32)
    else:
        out_spec = pl.BlockSpec((tm, tn), lambda i, j, k: (i, j))
        out_shape = jax.ShapeDtypeStruct((M, N), F32)
    return pl.pallas_call(
        body, name=name, grid=(M // tm, N // tn, nk),
        in_specs=[a_spec, b_spec], out_specs=out_spec, out_shape=out_shape,
        compiler_params=_params(("parallel", "parallel", "arbitrary"), VMEM_LIMIT),
    )(a, b)


def _rms(x, g):
    r = lax.rsqrt(jnp.mean(x * x, axis=-1, keepdims=True) + RMS_EPS)
    return x * r * g


def _rms_bwd(x, g, dy):
    r = lax.rsqrt(jnp.mean(x * x, axis=-1, keepdims=True) + RMS_EPS)
    n = x * r
    dn = dy * g
    dx = r * (dn - n * jnp.mean(dn * n, axis=-1, keepdims=True))
    return dx, jnp.sum(dy * n, axis=0, keepdims=True)


def _sigmoid(x):
    return 1.0 / (1.0 + jnp.exp(-x))


_GELU_C = math.sqrt(2.0 / math.pi)


def _gelu(x):
    return 0.5 * x * (1.0 + jnp.tanh(_GELU_C * (x + 0.044715 * x * x * x)))


def _gelu_grad(x):
    t = jnp.tanh(_GELU_C * (x + 0.044715 * x * x * x))
    return 0.5 * (1.0 + t) + 0.5 * x * (1.0 - t * t) * _GELU_C * (1.0 + 3.0 * 0.044715 * x * x)


def _row_spec(width, col_block=0):
    return pl.BlockSpec((ROW_TILE, width), lambda i: (i, col_block))


def _vec_spec(width, col_block=0):
    return pl.BlockSpec((1, width), lambda i: (0, col_block))


def _accumulate(ref, part):
    @pl.when(pl.program_id(0) == 0)
    def _():
        ref[...] = part

    @pl.when(pl.program_id(0) > 0)
    def _():
        ref[...] += part


def _rms_fwd_call(x, g):
    S, D = x.shape

    def body(x_ref, g_ref, h_ref):
        h_ref[...] = _rms(x_ref[...], g_ref[...]).astype(BF16)

    return pl.pallas_call(
        body, name="rms_mix_pre", grid=(S // ROW_TILE,),
        in_specs=[_row_spec(D), _vec_spec(D)], out_specs=_row_spec(D),
        out_shape=jax.ShapeDtypeStruct((S, D), BF16),
        compiler_params=_params(("parallel",)),
    )(x, g)


def _ln_silu_call(c1, g, b):
    S, C = c1.shape

    def body(c_ref, g_ref, b_ref, o_ref):
        xv = c_ref[...]
        mu = jnp.mean(xv, axis=-1, keepdims=True)
        xc = xv - mu
        var = jnp.mean(xc * xc, axis=-1, keepdims=True)
        z = xc * lax.rsqrt(var + LN_EPS) * g_ref[...] + b_ref[...]
        o_ref[...] = (z * _sigmoid(z)).astype(BF16)

    return pl.pallas_call(
        body, name="conv_ln_silu", grid=(S // ROW_TILE,),
        in_specs=[_row_spec(C), _vec_spec(C), _vec_spec(C)], out_specs=_row_spec(C),
        out_shape=jax.ShapeDtypeStruct((S, C), BF16),
        compiler_params=_params(("parallel",)),
    )(c1, g, b)


def _ln_silu_bwd_call(c1, g, b, dc):
    S, C = c1.shape

    def body(c_ref, g_ref, b_ref, dc_ref, dx_ref, dg_ref, db_ref):
        xv = c_ref[...]
        mu = jnp.mean(xv, axis=-1, keepdims=True)
        xc = xv - mu
        rs = lax.rsqrt(jnp.mean(xc * xc, axis=-1, keepdims=True) + LN_EPS)
        xh = xc * rs
        z = xh * g_ref[...] + b_ref[...]
        sg = _sigmoid(z)
        dz = dc_ref[...] * (sg * (1.0 + z * (1.0 - sg)))
        dxh = dz * g_ref[...]
        dx_ref[...] = rs * (dxh - jnp.mean(dxh, axis=-1, keepdims=True) - xh * jnp.mean(dxh * xh, axis=-1, keepdims=True))
        _accumulate(dg_ref, jnp.sum(dz * xh, axis=0, keepdims=True))
        _accumulate(db_ref, jnp.sum(dz, axis=0, keepdims=True))

    return pl.pallas_call(
        body, name="conv_ln_silu_bwd", grid=(S // ROW_TILE,),
        in_specs=[_row_spec(C), _vec_spec(C), _vec_spec(C), _row_spec(C)],
        out_specs=[_row_spec(C), _vec_spec(C), _vec_spec(C)],
        out_shape=[jax.ShapeDtypeStruct((S, C), F32), jax.ShapeDtypeStruct((1, C), F32), jax.ShapeDtypeStruct((1, C), F32)],
        compiler_params=_params(("arbitrary",)),
    )(c1, g, b, dc)


def _mix_call(proj, gate_col0, b_gate, y_a, y_c):
    S, D = y_a.shape
    w = 512
    nc = D // w
    ga0, gc0 = gate_col0 // w, (gate_col0 + D) // w

    def body(ga_ref, gc_ref, ba_ref, bc_ref, ya_ref, yc_ref, o_ref):
        o_ref[...] = (_sigmoid(ga_ref[...] + ba_ref[...]) * ya_ref[...]
                      + _sigmoid(gc_ref[...] + bc_ref[...]) * yc_ref[...]).astype(BF16)

    tile = lambda off: pl.BlockSpec((ROW_TILE, w), lambda i, j: (i, off + j))
    vec = lambda off: pl.BlockSpec((1, w), lambda i, j: (0, off + j))
    return pl.pallas_call(
        body, name="gate_mix", grid=(S // ROW_TILE, nc),
        in_specs=[tile(ga0), tile(gc0), vec(0), vec(nc), tile(0), tile(0)],
        out_specs=tile(0), out_shape=jax.ShapeDtypeStruct((S, D), BF16),
        compiler_params=_params(("parallel", "parallel")),
    )(proj, proj, b_gate, b_gate, y_a, y_c)


def _mix_bwd_call(dmixed, proj, gate_col0, b_gate, y_a, y_c):
    S, D = y_a.shape
    w = 512
    nc = D // w
    ga0, gc0 = gate_col0 // w, (gate_col0 + D) // w

    def body(dm_ref, ga_ref, gc_ref, ba_ref, bc_ref, ya_ref, yc_ref, dya_ref, dyc_ref, dga_ref, dgc_ref, dba_ref, dbc_ref):
        dm = dm_ref[...]
        sa = _sigmoid(ga_ref[...] + ba_ref[...])
        sc = _sigmoid(gc_ref[...] + bc_ref[...])
        dya_ref[...] = (dm * sa).astype(BF16)
        dyc_ref[...] = (dm * sc).astype(BF16)
        dga = dm * ya_ref[...] * sa * (1.0 - sa)
        dgc = dm * yc_ref[...] * sc * (1.0 - sc)
        dga_ref[...] = dga.astype(BF16)
        dgc_ref[...] = dgc.astype(BF16)
        pa = jnp.sum(dga, axis=0, keepdims=True)
        pc = jnp.sum(dgc, axis=0, keepdims=True)

        @pl.when(pl.program_id(1) == 0)
        def _():
            dba_ref[...] = pa
            dbc_ref[...] = pc

        @pl.when(pl.program_id(1) > 0)
        def _():
            dba_ref[...] += pa
            dbc_ref[...] += pc

    tile = lambda off: pl.BlockSpec((ROW_TILE, w), lambda j, i: (i, off + j))
    vec = lambda off: pl.BlockSpec((1, w), lambda j, i: (0, off + j))
    return pl.pallas_call(
        body, name="gate_mix_bwd", grid=(nc, S // ROW_TILE),
        in_specs=[tile(0), tile(ga0), tile(gc0), vec(0), vec(nc), tile(0), tile(0)],
        out_specs=[tile(0), tile(0), tile(0), tile(0), vec(0), vec(0)],
        out_shape=[jax.ShapeDtypeStruct((S, D), BF16)] * 4 + [
                   jax.ShapeDtypeStruct((1, D), F32), jax.ShapeDtypeStruct((1, D), F32)],
        compiler_params=_params(("parallel", "arbitrary")),
    )(dmixed, proj, proj, b_gate, b_gate, y_a, y_c)


def _res1_call(x, out, g_post, g_pre):
    S, D = x.shape

    def body(x_ref, o_ref, gp_ref, gq_ref, x1_ref, h2_ref):
        x1 = x_ref[...] + _rms(o_ref[...], gp_ref[...])
        x1_ref[...] = x1
        h2_ref[...] = _rms(x1, gq_ref[...]).astype(BF16)

    return pl.pallas_call(
        body, name="residual_mix", grid=(S // ROW_TILE,),
        in_specs=[_row_spec(D), _row_spec(D), _vec_spec(D), _vec_spec(D)],
        out_specs=[_row_spec(D), _row_spec(D)],
        out_shape=[jax.ShapeDtypeStruct((S, D), F32), jax.ShapeDtypeStruct((S, D), BF16)],
        compiler_params=_params(("parallel",)),
    )(x, out, g_post, g_pre)


def _loss_call(y, x1, g_post, target):
    S, D = y.shape

    def body(y_ref, x1_ref, g_ref, t_ref, loss_ref, dx_ref):
        err = x1_ref[...] + _rms(y_ref[...], g_ref[...]) - t_ref[...]
        dx_ref[...] = err * (1.0 / D)
        part = 0.5 * jnp.sum(jnp.mean(err * err, axis=-1, keepdims=True), axis=0, keepdims=True)
        _accumulate(loss_ref, jnp.broadcast_to(part, (SUBLANES, LANES)))

    return pl.pallas_call(
        body, name="residual_ffn_loss", grid=(S // ROW_TILE,),
        in_specs=[_row_spec(D), _row_spec(D), _vec_spec(D), _row_spec(D)],
        out_specs=[pl.BlockSpec((SUBLANES, LANES), lambda i: (0, 0)), _row_spec(D)],
        out_shape=[jax.ShapeDtypeStruct((SUBLANES, LANES), F32), jax.ShapeDtypeStruct((S, D), F32)],
        compiler_params=_params(("arbitrary",)),
    )(y, x1, g_post, target)


def _rms_bwd_call(x, g, dy, name):
    S, D = x.shape

    def body(x_ref, g_ref, dy_ref, dx_ref, dg_ref):
        dx, dg = _rms_bwd(x_ref[...], g_ref[...], dy_ref[...])
        dx_ref[...] = dx.astype(BF16)
        _accumulate(dg_ref, dg)

    return pl.pallas_call(
        body, name=name, grid=(S // ROW_TILE,),
        in_specs=[_row_spec(D), _vec_spec(D), _row_spec(D)],
        out_specs=[_row_spec(D), _vec_spec(D)],
        out_shape=[jax.ShapeDtypeStruct((S, D), BF16), jax.ShapeDtypeStruct((1, D), F32)],
        compiler_params=_params(("arbitrary",)),
    )(x, g, dy)


def _mid_bwd_call(x1, g_pre, dh2, dx2, out, g_post):
    S, D = x1.shape

    def body(x1_ref, gq_ref, dh_ref, dx2_ref, o_ref, gp_ref, dx1_ref, do_ref, dgq_ref, dgp_ref):
        d, dgq = _rms_bwd(x1_ref[...], gq_ref[...], dh_ref[...])
        dx1 = dx2_ref[...] + d
        dx1_ref[...] = dx1
        do, dgp = _rms_bwd(o_ref[...], gp_ref[...], dx1)
        do_ref[...] = do.astype(BF16)
        _accumulate(dgq_ref, dgq)
        _accumulate(dgp_ref, dgp)

    return pl.pallas_call(
        body, name="residual_mix_bwd", grid=(S // ROW_TILE,),
        in_specs=[_row_spec(D), _vec_spec(D), _row_spec(D), _row_spec(D), _row_spec(D), _vec_spec(D)],
        out_specs=[_row_spec(D), _row_spec(D), _vec_spec(D), _vec_spec(D)],
        out_shape=[jax.ShapeDtypeStruct((S, D), F32), jax.ShapeDtypeStruct((S, D), BF16)] + [jax.ShapeDtypeStruct((1, D), F32)] * 2,
        compiler_params=_params(("arbitrary",)),
    )(x1, g_pre, dh2, dx2, out, g_post)


def _in_bwd_call(x, g, dh1, dx1):
    S, D = x.shape

    def body(x_ref, g_ref, dh_ref, dx1_ref, gx_ref, dg_ref):
        d, dg = _rms_bwd(x_ref[...], g_ref[...], dh_ref[...])
        gx_ref[...] = dx1_ref[...] + d
        _accumulate(dg_ref, dg)

    return pl.pallas_call(
        body, name="rms_mix_pre_bwd", grid=(S // ROW_TILE,),
        in_specs=[_row_spec(D), _vec_spec(D), _row_spec(D), _row_spec(D)],
        out_specs=[_row_spec(D), _vec_spec(D)],
        out_shape=[jax.ShapeDtypeStruct((S, D), F32), jax.ShapeDtypeStruct((1, D), F32)],
        compiler_params=_params(("arbitrary",)),
    )(x, g, dh1, dx1)


def _bucket_table(dilation):
    qi = np.arange(SPAN)[:, None]
    ki = np.arange(2 * SPAN)[None, :]
    dist = np.maximum(qi + SPAN - ki, 0) * dilation
    max_exact = N_BUCKETS // 2
    d = np.maximum(dist, 1).astype(np.float64)
    large = max_exact + (np.log(d / max_exact) / math.log(MAX_DISTANCE / max_exact) * (N_BUCKETS - max_exact)).astype(np.int32)
    large = np.minimum(large, N_BUCKETS - 1)
    return np.where(dist < max_exact, dist, large).astype(np.int32)


def _bucket_tables():
    return jnp.asarray(np.stack([_bucket_table(r) for _, r in DILATED_PATTERNS]))


def _bias_table_call(rel_bias, buckets):
    def body(rb_ref, bk_ref, o_ref):
        for h in range(N_HEADS):
            bk = bk_ref[h // HEADS_PER_GROUP]

            def step(b, acc):
                return jnp.where(bk == b, rb_ref[b, h], acc)

            o_ref[h] = lax.fori_loop(0, N_BUCKETS, step, jnp.zeros((SPAN, 2 * SPAN), F32))

    return pl.pallas_call(
        body, name="rel_bias_table",
        in_specs=[pl.BlockSpec(memory_space=pltpu.SMEM), pl.BlockSpec(memory_space=pltpu.VMEM)],
        out_specs=pl.BlockSpec(memory_space=pltpu.VMEM),
        out_shape=jax.ShapeDtypeStruct((N_HEADS, SPAN, 2 * SPAN), F32),
    )(rel_bias, buckets)


def _bias_grad_call(dbias, buckets):
    def body(db_ref, bk_ref, o_ref, rows_ref):
        for h in range(N_HEADS):
            bk = bk_ref[h // HEADS_PER_GROUP]
            dv = db_ref[h]

            def step(b, carry):
                rows_ref[h, b] = jnp.sum(jnp.where(bk == b, dv, 0.0), axis=0, keepdims=True)
                return carry

            lax.fori_loop(0, N_BUCKETS, step, 0)
        o_ref[...] = jnp.sum(rows_ref[...], axis=-1, keepdims=True)

    out = pl.pallas_call(
        body, name="rel_bias_grad",
        in_specs=[pl.BlockSpec(memory_space=pltpu.VMEM), pl.BlockSpec(memory_space=pltpu.VMEM)],
        out_specs=pl.BlockSpec(memory_space=pltpu.VMEM),
        out_shape=jax.ShapeDtypeStruct((N_HEADS, N_BUCKETS, 1, 1), F32),
        scratch_shapes=[pltpu.VMEM((N_HEADS, N_BUCKETS, 1, 2 * SPAN), F32)],
    )(dbias, buckets)
    return out.reshape(N_HEADS, N_BUCKETS).T


def _dot_nt(a, b):
    return lax.dot_general(a, b, (((1,), (1,)), ((), ())), preferred_element_type=F32)


def _dot_nn(a, b):
    return lax.dot_general(a, b, (((1,), (0,)), ((), ())), preferred_element_type=F32)


def _dot_tn(a, b):
    return lax.dot_general(a, b, (((0,), (0,)), ((), ())), preferred_element_type=F32)


def _band_masks(n, nb):
    qi = lax.broadcasted_iota(jnp.int32, (SPAN, SPAN), 0)
    ki = lax.broadcasted_iota(jnp.int32, (SPAN, SPAN), 1)
    prev_ok = jnp.logical_and(ki >= qi, n > 0)
    cur_ok = ki <= qi
    next_ok = jnp.logical_and(ki >= qi, n < nb - 1)
    return prev_ok, cur_ok, next_ok


def _regroup_call(src, col_block, r, inverse, name):
    S = src.shape[0]
    L = S // r
    nt = GROUP_WIDTH // LANES

    def body(x_ref, o_ref):
        for rho in range(r):
            if inverse:
                o_ref[pl.ds(rho, L, stride=r), :] = x_ref[rho * L:(rho + 1) * L, :]
            else:
                o_ref[rho * L:(rho + 1) * L, :] = x_ref[pl.ds(rho, L, stride=r), :]

    return pl.pallas_call(
        body, name=name, grid=(nt,),
        in_specs=[pl.BlockSpec((S, LANES), lambda i: (0, col_block * nt + i))],
        out_specs=pl.BlockSpec((S, LANES), lambda i: (0, i)),
        out_shape=jax.ShapeDtypeStruct((S, GROUP_WIDTH), F32),
        compiler_params=_params(("parallel",)),
    )(src)


def _to_group_order(src, col_block, group, name):
    r = DILATED_PATTERNS[group][1]
    if r == 1:
        return src, col_block
    return _regroup_call(src, col_block, r, False, name), 0


def _to_token_order(arr, group, name):
    r = DILATED_PATTERNS[group][1]
    return arr if r == 1 else _regroup_call(arr, 0, r, True, name)


def _attn_fwd_call(q, k, v, bias, group):
    S = q[0].shape[0]
    r = DILATED_PATTERNS[group][1]
    nb = S // r // SPAN
    scale = HEAD_DIM ** -0.5

    def body(q_ref, kp_ref, kc_ref, vp_ref, vc_ref, b_ref, o_ref, lse_ref):
        n = pl.program_id(1)
        prev_ok, cur_ok, _ = _band_masks(n, nb)
        for j in range(HEADS_PER_GROUP):
            sl = slice(j * HEAD_DIM, (j + 1) * HEAD_DIM)
            q = q_ref[:, sl].astype(BF16)
            sp = _dot_nt(q, kp_ref[:, sl].astype(BF16)) * scale + b_ref[j, :, :SPAN]
            sc = _dot_nt(q, kc_ref[:, sl].astype(BF16)) * scale + b_ref[j, :, SPAN:]
            sp = jnp.where(prev_ok, sp, NEG_INF)
            sc = jnp.where(cur_ok, sc, NEG_INF)
            m = jnp.maximum(jnp.max(sp, axis=-1, keepdims=True), jnp.max(sc, axis=-1, keepdims=True))
            pp = jnp.exp(sp - m)
            pc = jnp.exp(sc - m)
            den = jnp.sum(pp, axis=-1, keepdims=True) + jnp.sum(pc, axis=-1, keepdims=True)
            o_ref[:, sl] = (_dot_nn(pp.astype(BF16), vp_ref[:, sl].astype(BF16))
                            + _dot_nn(pc.astype(BF16), vc_ref[:, sl].astype(BF16))) / den
            lse_ref[:, sl] = jnp.broadcast_to(m + jnp.log(den), (SPAN, HEAD_DIM))

    blk = (SPAN, GROUP_WIDTH)
    cur = lambda cb: pl.BlockSpec(blk, lambda rho, n: (rho * nb + n, cb))
    prev = lambda cb: pl.BlockSpec(blk, lambda rho, n: (rho * nb + jnp.maximum(n - 1, 0), cb))
    return pl.pallas_call(
        body, name=f"attn_fwd_g{group}", grid=(r, nb),
        in_specs=[cur(q[1]), prev(k[1]), cur(k[1]), prev(v[1]), cur(v[1]),
                  pl.BlockSpec((HEADS_PER_GROUP, SPAN, 2 * SPAN), lambda rho, n: (group, 0, 0))],
        out_specs=[cur(0)] * 2,
        out_shape=[jax.ShapeDtypeStruct((S, GROUP_WIDTH), F32)] * 2,
        compiler_params=_params(("parallel", "parallel")),
    )(q[0], k[0], k[0], v[0], v[0], bias)


def _attn_merge_call(parts):
    S = parts[0].shape[0]

    def body(o1, s1, o2, s2, o3, s3, a_ref, ab_ref, lse_ref):
        mx = jnp.maximum(jnp.maximum(s1[...], s2[...]), s3[...])
        w1 = jnp.exp(s1[...] - mx)
        w2 = jnp.exp(s2[...] - mx)
        w3 = jnp.exp(s3[...] - mx)
        den = w1 + w2 + w3
        a = (w1 * o1[...] + w2 * o2[...] + w3 * o3[...]) / den
        a_ref[...] = a
        ab_ref[...] = a.astype(BF16)
        lse_ref[...] = mx + jnp.log(den)

    return pl.pallas_call(
        body, name="attn_merge", grid=(S // ROW_TILE,),
        in_specs=[_row_spec(GROUP_WIDTH)] * 6, out_specs=[_row_spec(GROUP_WIDTH)] * 3,
        out_shape=[jax.ShapeDtypeStruct((S, GROUP_WIDTH), F32), jax.ShapeDtypeStruct((S, GROUP_WIDTH), BF16),
                   jax.ShapeDtypeStruct((S, GROUP_WIDTH), F32)],
        compiler_params=_params(("parallel",)),
    )(*parts)


def _attn_delta_call(a, da):
    S = a.shape[0]

    def body(a_ref, da_ref, d_ref):
        for j in range(HEADS_PER_GROUP):
            sl = slice(j * HEAD_DIM, (j + 1) * HEAD_DIM)
            d = jnp.sum(a_ref[:, sl] * da_ref[:, sl], axis=-1, keepdims=True)
            d_ref[:, sl] = jnp.broadcast_to(d, (ROW_TILE, HEAD_DIM))

    return pl.pallas_call(
        body, name="attn_delta", grid=(S // ROW_TILE,),
        in_specs=[_row_spec(GROUP_WIDTH)] * 2, out_specs=_row_spec(GROUP_WIDTH),
        out_shape=jax.ShapeDtypeStruct((S, GROUP_WIDTH), F32),
        compiler_params=_params(("parallel",)),
    )(a, da)


def _attn_bwd_call(q, k, v, bias, da, lse, delta, group):
    S = q[0].shape[0]
    r = DILATED_PATTERNS[group][1]
    nb = S // r // SPAN
    scale = HEAD_DIM ** -0.5

    def body(q_ref, qn_ref, kp_ref, kc_ref, vp_ref, vc_ref, b_ref, da_ref, dan_ref, lse_ref, lsen_ref, dl_ref, dln_ref,
             dq_ref, dk_ref, dv_ref, db_ref):
        n = pl.program_id(1)
        prev_ok, cur_ok, next_ok = _band_masks(n, nb)
        first = jnp.logical_and(pl.program_id(0) == 0, n == 0)
        for j in range(HEADS_PER_GROUP):
            sl = slice(j * HEAD_DIM, (j + 1) * HEAD_DIM)
            q = q_ref[:, sl].astype(BF16)
            qn = qn_ref[:, sl].astype(BF16)
            kp = kp_ref[:, sl].astype(BF16)
            kc = kc_ref[:, sl].astype(BF16)
            vp = vp_ref[:, sl].astype(BF16)
            vc = vc_ref[:, sl].astype(BF16)
            dav = da_ref[:, sl].astype(BF16)
            dan = dan_ref[:, sl].astype(BF16)
            bp = b_ref[j, :, :SPAN]
            bc = b_ref[j, :, SPAN:]
            pp = jnp.exp(jnp.where(prev_ok, _dot_nt(q, kp) * scale + bp, NEG_INF) - lse_ref[:, sl])
            pc = jnp.exp(jnp.where(cur_ok, _dot_nt(q, kc) * scale + bc, NEG_INF) - lse_ref[:, sl])
            pn = jnp.exp(jnp.where(next_ok, _dot_nt(qn, kc) * scale + bp, NEG_INF) - lsen_ref[:, sl])
            dsp = pp * (_dot_nt(dav, vp) - dl_ref[:, sl])
            dsc = pc * (_dot_nt(dav, vc) - dl_ref[:, sl])
            dsn = pn * (_dot_nt(dan, vc) - dln_ref[:, sl])
            dsp_b, dsc_b, dsn_b = dsp.astype(BF16), dsc.astype(BF16), dsn.astype(BF16)
            dq_ref[:, sl] = (_dot_nn(dsp_b, kp) + _dot_nn(dsc_b, kc)) * scale
            dk_ref[:, sl] = (_dot_tn(dsc_b, q) + _dot_tn(dsn_b, qn)) * scale
            dv_ref[:, sl] = _dot_tn(pc.astype(BF16), dav) + _dot_tn(pn.astype(BF16), dan)

            @pl.when(first)
            def _():
                db_ref[j, :, :SPAN] = dsp
                db_ref[j, :, SPAN:] = dsc

            @pl.when(jnp.logical_not(first))
            def _():
                db_ref[j, :, :SPAN] += dsp
                db_ref[j, :, SPAN:] += dsc

    blk = (SPAN, GROUP_WIDTH)
    cur = lambda cb: pl.BlockSpec(blk, lambda rho, n: (rho * nb + n, cb))
    prev = lambda cb: pl.BlockSpec(blk, lambda rho, n: (rho * nb + jnp.maximum(n - 1, 0), cb))
    nxt = lambda cb: pl.BlockSpec(blk, lambda rho, n: (rho * nb + jnp.minimum(n + 1, nb - 1), cb))
    band = (HEADS_PER_GROUP, SPAN, 2 * SPAN)
    return pl.pallas_call(
        body, name=f"attn_bwd_g{group}", grid=(r, nb),
        in_specs=[cur(q[1]), nxt(q[1]), prev(k[1]), cur(k[1]), prev(v[1]), cur(v[1]),
                  pl.BlockSpec(band, lambda rho, n: (group, 0, 0)),
                  cur(0), nxt(0), cur(0), nxt(0), cur(0), nxt(0)],
        out_specs=[cur(0), cur(0), cur(0), pl.BlockSpec(band, lambda rho, n: (0, 0, 0))],
        out_shape=[jax.ShapeDtypeStruct((S, GROUP_WIDTH), F32)] * 3 + [jax.ShapeDtypeStruct(band, F32)],
        compiler_params=_params(("arbitrary", "arbitrary")),
    )(q[0], q[0], k[0], k[0], v[0], v[0], bias, da, da, lse, lse, delta, delta)


def _taps(width):
    return [(k, (width - 1 - k) // SUBLANES, (width - 1 - k) % SUBLANES) for k in range(width)]


def _shifted(win, width, pad, up):
    total = win.shape[0]
    for b in range(SUBLANES):
        taps = [(k, a) for k, a, bb in _taps(width) if bb == b]
        if not taps:
            continue
        if up:
            rolled = win if b == 0 else pltpu.roll(win, total - b, axis=0)
        else:
            rolled = win if b == 0 else pltpu.roll(win, b, axis=0)
        for k, a in taps:
            start = SUBLANES * a if up else pad - SUBLANES * a
            yield k, rolled[start:start + TIME_BLOCK, :]


def _conv_block(win, w_ref, width, pad):
    acc = None
    for k, rows in _shifted(win, width, pad, up=False):
        term = w_ref[k:k + 1, :] * rows
        acc = term if acc is None else acc + term
    return acc


def _conv_transpose_block(win, w_ref, width, pad):
    acc = None
    for k, rows in _shifted(win, width, pad, up=True):
        term = w_ref[k:k + 1, :] * rows
        acc = term if acc is None else acc + term
    return acc


def _conv_weight_grad(win, dy, dw_ref, width, pad):
    for k, rows in _shifted(win, width, pad, up=False):
        dw_ref[k:k + 1, :] += jnp.sum(dy * rows, axis=0, keepdims=True)


def _time_loop(S, step):
    def it(tb, carry):
        step(pl.multiple_of(tb * TIME_BLOCK, TIME_BLOCK))
        return carry

    lax.fori_loop(0, S // TIME_BLOCK, it, 0)


def _conv_fwd_call(proj, col0, w, b):
    S = proj.shape[0]
    C = w.shape[1]
    nt = C // LANES
    v0, g0 = col0 // LANES, (col0 + C) // LANES

    def body(val_ref, gate_ref, w_ref, b_ref, o_ref, pad_ref):
        pad_ref[0:CONV_PAD, :] = jnp.zeros((CONV_PAD, LANES), F32)
        pad_ref[CONV_PAD:, :] = val_ref[...] * _sigmoid(gate_ref[...])

        def step(t0):
            win = pad_ref[pl.ds(t0, TIME_BLOCK + CONV_PAD), :]
            o_ref[pl.ds(t0, TIME_BLOCK), :] = _conv_block(win, w_ref, CONV_WIDTH, CONV_PAD) + b_ref[...]

        _time_loop(S, step)

    seq = lambda off: pl.BlockSpec((S, LANES), lambda i: (0, off + i))
    return pl.pallas_call(
        body, name="conv_module", grid=(nt,),
        in_specs=[seq(v0), seq(g0), pl.BlockSpec((CONV_WIDTH, LANES), lambda i: (0, i)), pl.BlockSpec((1, LANES), lambda i: (0, i))],
        out_specs=seq(0), out_shape=jax.ShapeDtypeStruct((S, C), F32),
        scratch_shapes=[pltpu.VMEM((S + CONV_PAD, LANES), F32)],
        compiler_params=_params(("parallel",)),
    )(proj, proj, w, b)


def _conv_bwd_call(proj, col0, w, dc1):
    S = proj.shape[0]
    C = w.shape[1]
    nt = C // LANES
    v0, g0 = col0 // LANES, (col0 + C) // LANES

    def body(val_ref, gate_ref, w_ref, dy_ref, dval_ref, dgate_ref, dw_ref, db_ref, xpad_ref, dpad_ref, dwacc_ref):
        xpad_ref[0:CONV_PAD, :] = jnp.zeros((CONV_PAD, LANES), F32)
        xpad_ref[CONV_PAD:, :] = val_ref[...] * _sigmoid(gate_ref[...])
        dpad_ref[0:S, :] = dy_ref[...]
        dpad_ref[S:, :] = jnp.zeros((CONV_PAD, LANES), F32)
        dwacc_ref[...] = jnp.zeros_like(dwacc_ref)

        def step(t0):
            rows = pl.ds(t0, TIME_BLOCK)
            _conv_weight_grad(xpad_ref[pl.ds(t0, TIME_BLOCK + CONV_PAD), :], dy_ref[rows, :], dwacc_ref, CONV_WIDTH, CONV_PAD)
            dc0 = _conv_transpose_block(dpad_ref[pl.ds(t0, TIME_BLOCK + CONV_PAD), :], w_ref, CONV_WIDTH, CONV_PAD)
            sg = _sigmoid(gate_ref[rows, :])
            dval_ref[rows, :] = (dc0 * sg).astype(BF16)
            dgate_ref[rows, :] = (dc0 * val_ref[rows, :] * sg * (1.0 - sg)).astype(BF16)

        _time_loop(S, step)
        dw_ref[...] = dwacc_ref[...]
        db_ref[...] = jnp.sum(dy_ref[...], axis=0, keepdims=True)

    seq = lambda off: pl.BlockSpec((S, LANES), lambda i: (0, off + i))
    return pl.pallas_call(
        body, name="conv_module_bwd", grid=(nt,),
        in_specs=[seq(v0), seq(g0), pl.BlockSpec((CONV_WIDTH, LANES), lambda i: (0, i)), seq(0)],
        out_specs=[seq(0), seq(0), pl.BlockSpec((CONV_PAD, LANES), lambda i: (0, i)), pl.BlockSpec((1, LANES), lambda i: (0, i))],
        out_shape=[jax.ShapeDtypeStruct((S, C), BF16), jax.ShapeDtypeStruct((S, C), BF16),
                   jax.ShapeDtypeStruct((CONV_PAD, C), F32), jax.ShapeDtypeStruct((1, C), F32)],
        scratch_shapes=[pltpu.VMEM((S + CONV_PAD, LANES), F32), pltpu.VMEM((S + CONV_PAD, LANES), F32),
                        pltpu.VMEM((CONV_PAD, LANES), F32)],
        compiler_params=_params(("parallel",)),
    )(proj, proj, w, dc1)


def _ffn_fwd_call(u, w, b):
    S, C2 = u.shape
    C = C2 // 2
    nt = C // LANES

    def body(ug_ref, uv_ref, wg_ref, wv_ref, bg_ref, bv_ref, f_ref, pg_ref, pv_ref):
        zeros = jnp.zeros((FFN_PAD, LANES), F32)
        pg_ref[0:FFN_PAD, :] = zeros
        pv_ref[0:FFN_PAD, :] = zeros
        pg_ref[FFN_PAD:, :] = ug_ref[...]
        pv_ref[FFN_PAD:, :] = uv_ref[...]

        def step(t0):
            win = pl.ds(t0, TIME_BLOCK + FFN_PAD)
            cg = _conv_block(pg_ref[win, :], wg_ref, FFN_CONV_WIDTH, FFN_PAD) + bg_ref[...]
            cv = _conv_block(pv_ref[win, :], wv_ref, FFN_CONV_WIDTH, FFN_PAD) + bv_ref[...]
            f_ref[pl.ds(t0, TIME_BLOCK), :] = (_gelu(cg) * cv).astype(BF16)

        _time_loop(S, step)

    seq = lambda off: pl.BlockSpec((S, LANES), lambda i: (0, off + i))
    wsp = lambda off: pl.BlockSpec((FFN_CONV_WIDTH, LANES), lambda i: (0, off + i))
    bsp = lambda off: pl.BlockSpec((1, LANES), lambda i: (0, off + i))
    return pl.pallas_call(
        body, name="ffn_conv_geglu", grid=(nt,),
        in_specs=[seq(0), seq(nt), wsp(0), wsp(nt), bsp(0), bsp(nt)],
        out_specs=seq(0), out_shape=jax.ShapeDtypeStruct((S, C), BF16),
        scratch_shapes=[pltpu.VMEM((S + FFN_PAD, LANES), F32)] * 2,
        compiler_params=_params(("parallel",)),
    )(u, u, w, w, b, b)


def _ffn_bwd_call(u, w, b, df):
    S, C2 = u.shape
    C = C2 // 2
    nt = C // LANES

    def body(ug_ref, uv_ref, wg_ref, wv_ref, bg_ref, bv_ref, df_ref,
             dug_ref, duv_ref, dwg_ref, dwv_ref, dbg_ref, dbv_ref,
             pg_ref, pv_ref, dg_ref, dv_ref, dwg_acc, dwv_acc, dbg_acc, dbv_acc):
        zeros = jnp.zeros((FFN_PAD, LANES), F32)
        pg_ref[0:FFN_PAD, :] = zeros
        pv_ref[0:FFN_PAD, :] = zeros
        pg_ref[FFN_PAD:, :] = ug_ref[...]
        pv_ref[FFN_PAD:, :] = uv_ref[...]
        dg_ref[S:, :] = zeros
        dv_ref[S:, :] = zeros
        dwg_acc[...] = jnp.zeros_like(dwg_acc)
        dwv_acc[...] = jnp.zeros_like(dwv_acc)
        dbg_acc[...] = jnp.zeros_like(dbg_acc)
        dbv_acc[...] = jnp.zeros_like(dbv_acc)

        def first(t0):
            win = pl.ds(t0, TIME_BLOCK + FFN_PAD)
            rows = pl.ds(t0, TIME_BLOCK)
            xg = pg_ref[win, :]
            xv = pv_ref[win, :]
            cg = _conv_block(xg, wg_ref, FFN_CONV_WIDTH, FFN_PAD) + bg_ref[...]
            cv = _conv_block(xv, wv_ref, FFN_CONV_WIDTH, FFN_PAD) + bv_ref[...]
            dfb = df_ref[rows, :]
            dcg = dfb * cv * _gelu_grad(cg)
            dcv = dfb * _gelu(cg)
            dg_ref[rows, :] = dcg
            dv_ref[rows, :] = dcv
            _conv_weight_grad(xg, dcg, dwg_acc, FFN_CONV_WIDTH, FFN_PAD)
            _conv_weight_grad(xv, dcv, dwv_acc, FFN_CONV_WIDTH, FFN_PAD)
            dbg_acc[...] += jnp.sum(dcg, axis=0, keepdims=True)
            dbv_acc[...] += jnp.sum(dcv, axis=0, keepdims=True)

        def second(t0):
            win = pl.ds(t0, TIME_BLOCK + FFN_PAD)
            rows = pl.ds(t0, TIME_BLOCK)
            dug_ref[rows, :] = _conv_transpose_block(dg_ref[win, :], wg_ref, FFN_CONV_WIDTH, FFN_PAD).astype(BF16)
            duv_ref[rows, :] = _conv_transpose_block(dv_ref[win, :], wv_ref, FFN_CONV_WIDTH, FFN_PAD).astype(BF16)

        _time_loop(S, first)
        _time_loop(S, second)
        dwg_ref[...] = dwg_acc[...]
        dwv_ref[...] = dwv_acc[...]
        dbg_ref[...] = dbg_acc[...]
        dbv_ref[...] = dbv_acc[...]

    seq = lambda off: pl.BlockSpec((S, LANES), lambda i: (0, off + i))
    wsp = lambda off: pl.BlockSpec((FFN_CONV_WIDTH, LANES), lambda i: (0, off + i))
    bsp = lambda off: pl.BlockSpec((1, LANES), lambda i: (0, off + i))
    return pl.pallas_call(
        body, name="ffn_conv_geglu_bwd", grid=(nt,),
        in_specs=[seq(0), seq(nt), wsp(0), wsp(nt), bsp(0), bsp(nt), seq(0)],
        out_specs=[seq(0), seq(0), pl.BlockSpec((SUBLANES, LANES), lambda i: (0, i)), pl.BlockSpec((SUBLANES, LANES), lambda i: (0, i)),
                   bsp(0), bsp(0)],
        out_shape=[jax.ShapeDtypeStruct((S, C), BF16)] * 2 + [jax.ShapeDtypeStruct((SUBLANES, C), F32)] * 2
        + [jax.ShapeDtypeStruct((1, C), F32)] * 2,
        scratch_shapes=[pltpu.VMEM((S + FFN_PAD, LANES), F32)] * 4 + [pltpu.VMEM((SUBLANES, LANES), F32)] * 2
        + [pltpu.VMEM((1, LANES), F32)] * 2,
        compiler_params=_params(("parallel",)),
    )(u, u, w, w, b, b, df)


def _adamw_call(w, g, m, v, name):
    R, C = w.shape
    tr = _row_tile(R, C)
    c1 = 1.0 / (1.0 - ADAM_B1 ** ADAM_STEP)
    c2 = 1.0 / (1.0 - ADAM_B2 ** ADAM_STEP)

    def body(w_ref, g_ref, m_ref, v_ref, d_ref, mo_ref, vo_ref):
        gv = g_ref[...]
        mn = ADAM_B1 * m_ref[...] + (1.0 - ADAM_B1) * gv
        vn = ADAM_B2 * v_ref[...] + (1.0 - ADAM_B2) * (gv * gv)
        mo_ref[...] = mn
        vo_ref[...] = vn
        d_ref[...] = -ADAM_LR * ((mn * c1) / (jnp.sqrt(vn * c2) + ADAM_EPS) + ADAM_WD * w_ref[...])

    spec = pl.BlockSpec((tr, C), lambda i: (i, 0))
    return pl.pallas_call(
        body, name=name, grid=(R // tr,),
        in_specs=[spec] * 4, out_specs=[spec] * 3,
        out_shape=[jax.ShapeDtypeStruct((R, C), F32)] * 3,
        compiler_params=_params(("parallel",)),
    )(w, g, m, v)


def _position():
    return lax.axis_index("x"), lax.axis_index("y"), lax.axis_index("c")


def _chip_peers(x, y):
    return [(x, 1 - y), (1 - x, y), (1 - x, 1 - y)]


def _half_rows(ref, core, rows):
    h = rows // 2
    start = pl.multiple_of(core * h, 16)
    return ref.at[pl.ds(start, h), :] if len(ref.shape) == 2 else ref.at[:, pl.ds(start, h), :]


def _shard_half(ref, shard, core, rows):
    h = rows // 2
    return ref.at[shard, pl.ds(pl.multiple_of(core * h, 16), h), :]


ANY = pl.BlockSpec(memory_space=pl.ANY)


def _allgather_call(shards, whole):
    n, nw = len(shards), len(whole)
    outs_shape = [jax.ShapeDtypeStruct((N_CHIPS,) + s.shape, s.dtype) for s in shards + whole]

    def body(*refs):
        ins, outs = refs[:n + nw], refs[n + nw:2 * (n + nw)]
        send_sems, recv_sems, pass_send, pass_recv, own_send, own_recv = refs[2 * (n + nw):]
        x, y, c = _position()
        chip = 2 * x + y
        peers = _chip_peers(x, y)
        sent, local = [], []
        for i in range(n + nw):
            cp = pltpu.make_async_remote_copy(src_ref=ins[i], dst_ref=outs[i].at[chip], send_sem=own_send.at[i],
                                              recv_sem=own_recv.at[i], device_id=(x, y, 1 - c), device_id_type=MESH)
            cp.start()
            local.append(cp)
            rows = ins[i].shape[0]
            for k, (px, py) in enumerate(peers):
                if i < n:
                    src, dst = _half_rows(ins[i], c, rows), _shard_half(outs[i], chip, c, rows)
                else:
                    src, dst = ins[i], outs[i].at[chip]
                cp = pltpu.make_async_remote_copy(src_ref=src, dst_ref=dst, send_sem=send_sems.at[i, k],
                                                  recv_sem=recv_sems.at[i, k], device_id=(px, py, c), device_id_type=MESH)
                cp.start()
                sent.append(cp)
        passed = []
        for i in range(n + nw):
            rows = ins[i].shape[0]
            for k, (px, py) in enumerate(peers):
                landed = _shard_half(outs[i], 2 * px + py, c, rows) if i < n else outs[i].at[2 * px + py]
                pltpu.make_async_remote_copy(src_ref=landed, dst_ref=landed, send_sem=send_sems.at[i, k],
                                             recv_sem=recv_sems.at[i, k], device_id=(px, py, c), device_id_type=MESH).wait_recv()
                if i < n:
                    cp = pltpu.make_async_remote_copy(src_ref=landed, dst_ref=landed, send_sem=pass_send.at[i, k],
                                                      recv_sem=pass_recv.at[i, k], device_id=(x, y, 1 - c), device_id_type=MESH)
                    cp.start()
                    passed.append(cp)
        for cp in sent:
            cp.wait_send()
        for cp in passed:
            cp.wait()
        for cp in local:
            cp.wait()

    return pl.pallas_call(
        body, name="weight_allgather",
        in_specs=[ANY] * (n + nw), out_specs=[ANY] * (n + nw), out_shape=outs_shape,
        scratch_shapes=[pltpu.SemaphoreType.DMA((n + nw, 3)), pltpu.SemaphoreType.DMA((n + nw, 3)),
                        pltpu.SemaphoreType.DMA((n, 3)), pltpu.SemaphoreType.DMA((n, 3)),
                        pltpu.SemaphoreType.DMA((n + nw,)), pltpu.SemaphoreType.DMA((n + nw,))],
    )(*shards, *whole)


HBM_SPEC = pl.BlockSpec(memory_space=pltpu.HBM)
SEM_SPEC = pl.BlockSpec(memory_space=pltpu.SEMAPHORE)
DATAFLOW = pltpu.SideEffectType.DATAFLOW_SIDE_EFFECTING


def _in_hbm(a):
    return pltpu.with_memory_space_constraint(a, pltpu.HBM)


def _split_start(name, srcs, lands, n_sems, copies, after):
    n, m = len(srcs), len(lands)

    def body(*refs):
        src_refs, land_refs = refs[:n], refs[n:n + m]
        send_sem, recv_sem = refs[n + m + 1], refs[n + m + 2]
        token = refs[-1]
        for src, dst, dev, idx in copies(src_refs, land_refs):
            pltpu.make_async_remote_copy(src_ref=src, dst_ref=dst, send_sem=send_sem.at[idx], recv_sem=recv_sem.at[idx],
                                         device_id=dev, device_id_type=MESH).start()
        token[...] = jnp.zeros_like(token)

    outs = pl.pallas_call(
        body, name=name,
        in_specs=[HBM_SPEC] * (n + m) + [ANY],
        out_specs=[SEM_SPEC, SEM_SPEC] + [HBM_SPEC] * (n + m) + [pl.BlockSpec(memory_space=pltpu.VMEM)],
        out_shape=[pltpu.SemaphoreType.DMA((n_sems,)), pltpu.SemaphoreType.DMA((n_sems,))]
        + [pltpu.HBM(a.shape, a.dtype) for a in list(srcs) + list(lands)] + [jax.ShapeDtypeStruct((SUBLANES, LANES), F32)],
        input_output_aliases={i: 2 + i for i in range(n + m)},
        compiler_params=pltpu.CompilerParams(has_side_effects=DATAFLOW),
    )(*[_in_hbm(a) for a in list(srcs) + list(lands)], after)
    return dict(send=outs[0], recv=outs[1], srcs=list(outs[2:2 + n]), lands=list(outs[2 + n:2 + n + m]),
                tile=outs[-1], token=outs[-1][0, 0])


def _split_wait(name, started, copies, after):
    n, m = len(started["srcs"]), len(started["lands"])

    def body(*refs):
        src_refs, land_refs = refs[:n], refs[n:n + m]
        send_sem, recv_sem = refs[n + m], refs[n + m + 1]
        for src, dst, dev, idx in copies(src_refs, land_refs):
            cp = pltpu.make_async_remote_copy(src_ref=src, dst_ref=dst, send_sem=send_sem.at[idx], recv_sem=recv_sem.at[idx],
                                              device_id=dev, device_id_type=MESH)
            cp.wait_send()
            cp.wait_recv()

    arrays = started["srcs"] + started["lands"]
    outs = pl.pallas_call(
        body, name=name,
        in_specs=[HBM_SPEC] * (n + m) + [SEM_SPEC, SEM_SPEC, ANY],
        out_specs=[HBM_SPEC] * (n + m),
        out_shape=[pltpu.HBM(a.shape, a.dtype) for a in arrays],
        input_output_aliases={i: i for i in range(n + m)},
        compiler_params=pltpu.CompilerParams(has_side_effects=DATAFLOW),
    )(*arrays, started["send"], started["recv"], after)
    return list(outs)


def _gather_copies(srcs, lands):
    x, y, c = _position()
    chip = 2 * x + y
    targets = [(px, py, c) for px, py in _chip_peers(x, y)] + [(x, y, 1 - c)]
    return [(s, l.at[chip], dev, len(targets) * i + k) for i, (s, l) in enumerate(zip(srcs, lands)) for k, dev in enumerate(targets)]


def _sibling_copies(srcs, lands):
    x, y, c = _position()
    return [(_half_rows(srcs[0], 1 - c, srcs[0].shape[1]), lands[0], (x, y, 1 - c), 0)]


def _exchange_copies(srcs, lands):
    x, y, c = _position()
    return [(srcs[0].at[2 * px + py], lands[0].at[k], (px, py, c), k) for k, (px, py) in enumerate(_chip_peers(x, y))]


def _sibling_exchange_call(grads, name="grad_sibling_exchange"):
    n = len(grads)
    halves = [jax.ShapeDtypeStruct((N_CHIPS, g.shape[1] // 2, g.shape[2]), F32) for g in grads]

    def body(*refs):
        ins, outs = refs[:n], refs[n:2 * n]
        send_sems, recv_sems = refs[2 * n:]
        x, y, c = _position()
        copies = []
        for i in range(n):
            cp = pltpu.make_async_remote_copy(src_ref=_half_rows(ins[i], 1 - c, grads[i].shape[1]), dst_ref=outs[i],
                                              send_sem=send_sems.at[i], recv_sem=recv_sems.at[i],
                                              device_id=(x, y, 1 - c), device_id_type=MESH)
            cp.start()
            copies.append(cp)
        for cp in copies:
            cp.wait()

    return pl.pallas_call(
        body, name=name,
        in_specs=[ANY] * n, out_specs=[ANY] * n, out_shape=halves,
        scratch_shapes=[pltpu.SemaphoreType.DMA((n,)), pltpu.SemaphoreType.DMA((n,))],
    )(*grads)


def _pair_sum_call(grad, recv, core, name):
    _, h, B = recv.shape
    tr = _row_tile(h, B)

    def body(core_ref, g_ref, r_ref, o_ref, ob_ref):
        s = g_ref[...] + r_ref[...]
        o_ref[...] = s
        ob_ref[...] = s.astype(BF16)

    g_spec = pl.BlockSpec((None, tr, B), lambda q, i, core_ref: (q, core_ref[0] * (h // tr) + i, 0))
    spec = pl.BlockSpec((None, tr, B), lambda q, i, core_ref: (q, i, 0))
    return pl.pallas_call(
        body, name=name,
        grid_spec=pltpu.PrefetchScalarGridSpec(num_scalar_prefetch=1, grid=(N_CHIPS, h // tr), in_specs=[g_spec, spec],
                                               out_specs=[spec, spec]),
        out_shape=[jax.ShapeDtypeStruct(recv.shape, F32), jax.ShapeDtypeStruct(recv.shape, BF16)],
        compiler_params=_params(("parallel", "parallel")),
    )(core, grad, recv)


def _chip_exchange_call(partials):
    n = len(partials)
    outs_shape = [jax.ShapeDtypeStruct((3,) + t.shape[1:], BF16) for t in partials]

    def body(*refs):
        ins, outs = refs[:n], refs[n:2 * n]
        send_sems, recv_sems = refs[2 * n:]
        x, y, c = _position()
        copies = []
        for i in range(n):
            for k, (px, py) in enumerate(_chip_peers(x, y)):
                cp = pltpu.make_async_remote_copy(src_ref=ins[i].at[2 * px + py], dst_ref=outs[i].at[k], send_sem=send_sems.at[i, k],
                                                  recv_sem=recv_sems.at[i, k], device_id=(px, py, c), device_id_type=MESH)
                cp.start()
                copies.append(cp)
        for cp in copies:
            cp.wait()

    return pl.pallas_call(
        body, name="grad_chip_exchange",
        in_specs=[ANY] * n, out_specs=[ANY] * n, out_shape=outs_shape,
        scratch_shapes=[pltpu.SemaphoreType.DMA((n, 3)), pltpu.SemaphoreType.DMA((n, 3))],
    )(*partials)


def _chip_sum_call(partial, recv, chip_core, name):
    _, h, B = recv.shape
    tr = _row_tile(h, B)

    def body(cc_ref, p_ref, r_ref, o_ref):
        o_ref[...] = ((p_ref[...] + r_ref[0].astype(F32)) + r_ref[1].astype(F32)) + r_ref[2].astype(F32)

    return pl.pallas_call(
        body, name=name,
        grid_spec=pltpu.PrefetchScalarGridSpec(
            num_scalar_prefetch=1, grid=(h // tr,),
            in_specs=[pl.BlockSpec((None, tr, B), lambda i, cc_ref: (cc_ref[0], i, 0)),
                      pl.BlockSpec((3, tr, B), lambda i, cc_ref: (0, i, 0))],
            out_specs=pl.BlockSpec((tr, B), lambda i, cc_ref: (cc_ref[1] * (h // tr) + i, 0))),
        out_shape=jax.ShapeDtypeStruct((2 * h, B), F32),
        compiler_params=_params(("parallel",)),
    )(chip_core, partial, recv)


def _sibling_assemble_call(shards, name="grad_sibling_assemble"):
    n = len(shards)

    def body(*refs):
        ins, outs = refs[:n], refs[n:2 * n]
        send_sems, recv_sems = refs[2 * n:]
        x, y, c = _position()
        copies = []
        for i in range(n):
            rows = shards[i].shape[0]
            cp = pltpu.make_async_remote_copy(src_ref=_half_rows(ins[i], c, rows), dst_ref=_half_rows(outs[i], c, rows),
                                              send_sem=send_sems.at[i], recv_sem=recv_sems.at[i],
                                              device_id=(x, y, 1 - c), device_id_type=MESH)
            cp.start()
            copies.append(cp)
        for cp in copies:
            cp.wait()

    return pl.pallas_call(
        body, name=name,
        in_specs=[ANY] * n, out_specs=[ANY] * n,
        out_shape=[jax.ShapeDtypeStruct(s.shape, F32) for s in shards],
        input_output_aliases={i: i for i in range(n)},
        scratch_shapes=[pltpu.SemaphoreType.DMA((n,)), pltpu.SemaphoreType.DMA((n,))],
    )(*shards)


def _small_allreduce_call(packed):
    rows = packed.shape[0]

    def body(x_ref, o_ref, buf_ref, send_sems, recv_sems):
        x, y, c = _position()
        me = 4 * x + 2 * y + c
        buf_ref[me] = x_ref[...]
        copies = []
        for k in range(1, 8):
            peer = (1 - x if k & 4 else x, 1 - y if k & 2 else y, 1 - c if k & 1 else c)
            cp = pltpu.make_async_remote_copy(src_ref=buf_ref.at[me], dst_ref=buf_ref.at[me], send_sem=send_sems.at[k - 1],
                                              recv_sem=recv_sems.at[k - 1], device_id=peer, device_id_type=MESH)
            cp.start()
            copies.append(cp)
        for cp in copies:
            cp.wait()
        acc = buf_ref[0]
        for d in range(1, 8):
            acc = acc + buf_ref[d]
        o_ref[...] = acc

    return pl.pallas_call(
        body, name="small_grad_allreduce",
        in_specs=[pl.BlockSpec(memory_space=pltpu.VMEM)], out_specs=pl.BlockSpec(memory_space=pltpu.VMEM),
        out_shape=jax.ShapeDtypeStruct((rows, LANES), F32),
        scratch_shapes=[pltpu.VMEM((8, rows, LANES), F32), pltpu.SemaphoreType.DMA((7,)), pltpu.SemaphoreType.DMA((7,))],
    )(packed)


def _pack(arrays):
    flat = jnp.concatenate([a.reshape(-1).astype(F32) for a in arrays])
    rows = -(-flat.shape[0] // LANES)
    rows = -(-rows // SUBLANES) * SUBLANES
    flat = jnp.pad(flat, (0, rows * LANES - flat.shape[0]))
    return flat.reshape(rows, LANES)


def _unpack(packed, shapes):
    flat = packed.reshape(-1)
    out, off = [], 0
    for shp in shapes:
        size = int(np.prod(shp))
        out.append(flat[off:off + size].reshape(shp))
        off += size
    return out


def _local_step(xs, target, P, late_weights, on_grad):
    S, D = xs.shape
    qkv_width = 3 * N_HEADS * HEAD_DIM
    glu_col0, gate_col0 = qkv_width, qkv_width + 2 * D
    shard_major = lambda g: g.reshape(N_CHIPS, g.shape[0] // N_CHIPS, g.shape[1])

    h1 = _rms_fwd_call(xs, P["norm_mix_pre"])
    proj = _matmul(h1, P["w_in"], "nn", "proj_in")
    buckets = _bucket_tables()
    bias = _bias_table_call(P["rel_bias"], buckets)
    qkv, parts = [], []
    for g in range(N_GROUPS):
        q, k, v = [_to_group_order(proj, 3 * t + g, g, f"group_order_{'qkv'[t]}{g}") for t in range(3)]
        qkv.append((q, k, v))
        o_g, lse_g = _attn_fwd_call(q, k, v, bias, g)
        parts += [_to_token_order(o_g, g, f"token_order_o{g}"), _to_token_order(lse_g, g, f"token_order_lse{g}")]
    a, a_bf, lse = _attn_merge_call(parts)
    P = dict(P, **late_weights("mix", a_bf))
    y_a = _matmul(a_bf, P["w_attn_out"], "nn", "attn_out")
    c1 = _conv_fwd_call(proj, glu_col0, P["conv_dw_w"], P["conv_dw_b"])
    cact = _ln_silu_call(c1, P["conv_ln_g"], P["conv_ln_b"])
    y_c = _matmul(cact, P["conv_pw_w"], "nn", "conv_pw")
    mixed = _mix_call(proj, gate_col0, P["b_gate"], y_a, y_c)
    out = _matmul(mixed, P["w_out"], "nn", "mix_out")
    x1, h2 = _res1_call(xs, out, P["norm_mix_post"], P["norm_ffn_pre"])
    P = dict(P, **late_weights("ffn", h2))
    u = _matmul(h2, P["w_up"], "nn", "ffn_up")
    f = _ffn_fwd_call(u, P["ffn_conv_w"], P["ffn_conv_b"])
    yff = _matmul(f, P["w_down"], "nn", "ffn_down")
    loss_tile, dx2 = _loss_call(yff, x1, P["norm_ffn_post"], target)

    G = {}
    dyff, G["norm_ffn_post"] = _rms_bwd_call(yff, P["norm_ffn_post"], dx2, "rms_ffn_post_bwd")
    zero = on_grad("w_down", shard_major(_matmul(f, dyff, "tn", "ffn_down_dw")))
    df = _matmul(dyff, P["w_down"], "nt", "ffn_down_dx")
    dug, duv, dwg, dwv, dbg, dbv = _ffn_bwd_call(u, P["ffn_conv_w"], P["ffn_conv_b"] + zero, df)
    G["ffn_conv_w"] = jnp.concatenate([dwg[:FFN_CONV_WIDTH], dwv[:FFN_CONV_WIDTH]], axis=1)
    G["ffn_conv_b"] = jnp.concatenate([dbg, dbv], axis=1)
    du = jnp.concatenate([dug, duv], axis=1)
    zero = on_grad("w_up", _matmul(h2, du, "tn", "ffn_up_dw", out_shards=True))
    dh2 = _matmul(du, P["w_up"], "nt", "ffn_up_dx")
    dx1, dout, G["norm_ffn_pre"], G["norm_mix_post"] = _mid_bwd_call(x1, P["norm_ffn_pre"] + zero, dh2, dx2, out, P["norm_mix_post"])
    zero = on_grad("w_out", shard_major(_matmul(mixed, dout, "tn", "mix_out_dw")))
    dmixed = _matmul(dout, P["w_out"], "nt", "mix_out_dx")
    dya, dyc, dga, dgc, dba, dbc = _mix_bwd_call(dmixed, proj, gate_col0, P["b_gate"] + zero, y_a, y_c)
    G["b_gate"] = jnp.concatenate([dba, dbc], axis=1)
    zero = on_grad("w_attn_out", _matmul(a_bf, dya, "tn", "attn_out_dw", out_shards=True))
    zero = zero + on_grad("conv_pw_w", shard_major(_matmul(cact, dyc, "tn", "conv_pw_dw")))
    da = _matmul(dya, P["w_attn_out"], "nt", "attn_out_dx")
    dcact = _matmul(dyc, P["conv_pw_w"], "nt", "conv_pw_dx")
    dc1, G["conv_ln_g"], G["conv_ln_b"] = _ln_silu_bwd_call(c1, P["conv_ln_g"] + zero, P["conv_ln_b"], dcact)
    dval, dgate, dw_dw, G["conv_dw_b"] = _conv_bwd_call(proj, glu_col0, P["conv_dw_w"], dc1)
    G["conv_dw_w"] = dw_dw[:CONV_WIDTH]
    delta = _attn_delta_call(a, da)
    dqs, dks, dvs, dbs = [], [], [], []
    for g in range(N_GROUPS):
        grouped = [_to_group_order(t, 0, g, f"group_order_{n}{g}")[0] for t, n in ((da, "da"), (lse, "lse"), (delta, "delta"))]
        dq, dk, dv, db = _attn_bwd_call(*qkv[g], bias, *grouped, g)
        dqs.append(_to_token_order(dq, g, f"token_order_dq{g}"))
        dks.append(_to_token_order(dk, g, f"token_order_dk{g}"))
        dvs.append(_to_token_order(dv, g, f"token_order_dv{g}"))
        dbs.append(db)
    G["rel_bias"] = _bias_grad_call(jnp.concatenate(dbs, axis=0), buckets)
    dproj = jnp.concatenate([t.astype(BF16) for t in dqs + dks + dvs] + [dval, dgate, dga, dgc], axis=1)
    zero = on_grad("w_in", _matmul(h1, dproj, "tn", "proj_in_dw", out_shards=True, tm=512))
    dh1 = _matmul(dproj, P["w_in"], "nt", "proj_in_dx")
    grad_x, G["norm_mix_pre"] = _in_bwd_call(xs, P["norm_mix_pre"] + zero, dh1, dx1)
    return loss_tile, grad_x, G


def kernel(x, w_in, b_gate, rel_bias, w_attn_out, conv_dw_w, conv_dw_b, conv_ln_g, conv_ln_b, conv_pw_w, w_out, norm_mix_pre, norm_mix_post, norm_ffn_pre, norm_ffn_post, w_up, ffn_conv_w, ffn_conv_b, w_down, loss_target, m_w_in, m_b_gate, m_rel_bias, m_w_attn_out, m_conv_dw_w, m_conv_dw_b, m_conv_ln_g, m_conv_ln_b, m_conv_pw_w, m_w_out, m_norm_mix_pre, m_norm_mix_post, m_norm_ffn_pre, m_norm_ffn_post, m_w_up, m_ffn_conv_w, m_ffn_conv_b, m_w_down, v_w_in, v_b_gate, v_rel_bias, v_w_attn_out, v_conv_dw_w, v_conv_dw_b, v_conv_ln_g, v_conv_ln_b, v_conv_pw_w, v_w_out, v_norm_mix_pre, v_norm_mix_post, v_norm_ffn_pre, v_norm_ffn_post, v_w_up, v_ffn_conv_w, v_ffn_conv_b, v_w_down):
    weights = dict(w_in=w_in, b_gate=b_gate, rel_bias=rel_bias, w_attn_out=w_attn_out, conv_dw_w=conv_dw_w, conv_dw_b=conv_dw_b,
                   conv_ln_g=conv_ln_g, conv_ln_b=conv_ln_b, conv_pw_w=conv_pw_w, w_out=w_out, norm_mix_pre=norm_mix_pre,
                   norm_mix_post=norm_mix_post, norm_ffn_pre=norm_ffn_pre, norm_ffn_post=norm_ffn_post, w_up=w_up,
                   ffn_conv_w=ffn_conv_w, ffn_conv_b=ffn_conv_b, w_down=w_down)
    m_in = dict(w_in=m_w_in, b_gate=m_b_gate, rel_bias=m_rel_bias, w_attn_out=m_w_attn_out, conv_dw_w=m_conv_dw_w,
                conv_dw_b=m_conv_dw_b, conv_ln_g=m_conv_ln_g, conv_ln_b=m_conv_ln_b, conv_pw_w=m_conv_pw_w, w_out=m_w_out,
                norm_mix_pre=m_norm_mix_pre, norm_mix_post=m_norm_mix_post, norm_ffn_pre=m_norm_ffn_pre,
                norm_ffn_post=m_norm_ffn_post, w_up=m_w_up, ffn_conv_w=m_ffn_conv_w, ffn_conv_b=m_ffn_conv_b, w_down=m_w_down)
    v_in = dict(w_in=v_w_in, b_gate=v_b_gate, rel_bias=v_rel_bias, w_attn_out=v_w_attn_out, conv_dw_w=v_conv_dw_w,
                conv_dw_b=v_conv_dw_b, conv_ln_g=v_conv_ln_g, conv_ln_b=v_conv_ln_b, conv_pw_w=v_conv_pw_w, w_out=v_w_out,
                norm_mix_pre=v_norm_mix_pre, norm_mix_post=v_norm_mix_post, norm_ffn_pre=v_norm_ffn_pre,
                norm_ffn_post=v_norm_ffn_post, w_up=v_w_up, ffn_conv_w=v_ffn_conv_w, ffn_conv_b=v_ffn_conv_b, w_down=v_w_down)
    names = list(weights)
    xi, yi, ci = _position()
    chip = 2 * xi + yi
    core_arr = jnp.reshape(ci, (1,)).astype(jnp.int32)

    xs = x[0]
    target = loss_target[0]
    S, D = xs.shape

    big = ["w_in", "w_attn_out", "conv_pw_w", "w_out", "w_up", "w_down"]
    row_sharded = ("conv_pw_w", "w_out", "w_down")
    bf16_shard = {k: weights[k][0].astype(BF16) for k in big}
    natural = lambda k, g: g.reshape(-1, g.shape[2]) if k in row_sharded else g
    w_in_full, dw4, fc4 = _allgather_call([bf16_shard["w_in"]], [conv_dw_w[0], ffn_conv_w[0]])
    late_sets = dict(mix=["w_attn_out", "conv_pw_w", "w_out"], ffn=["w_up", "w_down"])
    started, after = {}, w_in_full
    for tag, keys in late_sets.items():
        srcs = [bf16_shard[k] for k in keys]
        lands = [lax.empty((N_CHIPS,) + s.shape, BF16) for s in srcs]
        started[tag] = _split_start(f"gather_{tag}_start", srcs, lands, 4 * len(keys), _gather_copies, after)
        after = started[tag]["tile"]
    launched = started["mix"]["token"] + started["ffn"]["token"]

    def late_weights(tag, after):
        landed = _split_wait(f"gather_{tag}_wait", started[tag], _gather_copies, after)[len(late_sets[tag]):]
        return {k: natural(k, g) for k, g in zip(late_sets[tag], landed)}

    chip_core = jnp.stack([chip, ci]).astype(jnp.int32)
    exchanging, pending = {}, {}

    def pair_up(after):
        token = jnp.float32(0.0)
        for k in list(exchanging):
            g3, r1 = _split_wait(f"sibling_exchange_wait_{k}", exchanging.pop(k), _sibling_copies, after)
            s32, s16 = _pair_sum_call(g3, r1, core_arr, f"pair_sum_{k}")
            land = lax.empty((3,) + s16.shape[1:], BF16)
            pending[k] = (s32, _split_start(f"chip_exchange_start_{k}", [s16], [land], 3, _exchange_copies, s32))
            token = token + pending[k][1]["token"]
        return token

    def on_grad(k, g3):
        token = pair_up(g3[0, :SUBLANES, :LANES])
        land = lax.empty((N_CHIPS, g3.shape[1] // 2, g3.shape[2]), F32)
        exchanging[k] = _split_start(f"sibling_exchange_start_{k}", [g3], [land], 1, _sibling_copies, core_arr)
        return token + exchanging[k]["token"]

    def finish(keys, after, tag):
        halves = []
        for k in keys:
            s32, st = pending[k]
            recv2 = _split_wait(f"chip_exchange_wait_{k}", st, _exchange_copies, after)[1]
            halves.append(_chip_sum_call(s32, recv2, chip_core, f"chip_sum_{k}"))
        return dict(zip(keys, _sibling_assemble_call(halves, f"grad_sibling_assemble_{tag}")))

    P = dict(w_in=w_in_full, conv_dw_w=jnp.concatenate(list(dw4), axis=1), ffn_conv_w=jnp.concatenate(list(fc4), axis=1),
             b_gate=b_gate, rel_bias=rel_bias, conv_dw_b=conv_dw_b, conv_ln_g=conv_ln_g, conv_ln_b=conv_ln_b,
             norm_mix_pre=norm_mix_pre + launched, norm_mix_post=norm_mix_post, norm_ffn_pre=norm_ffn_pre,
             norm_ffn_post=norm_ffn_post, ffn_conv_b=ffn_conv_b)
    loss_tile, grad_x, G = _local_step(xs, target, P, late_weights, on_grad)
    loss = lax.psum(loss_tile[0, 0], ("x", "y", "c"))
    launched = pair_up(grad_x[:SUBLANES, :LANES])
    G["norm_mix_pre"] = G["norm_mix_pre"] + launched

    small = [k for k in names if k not in big]
    packed = _pack([G[k] for k in small])
    summed_block = _small_allreduce_call(packed)
    summed = _unpack(summed_block, [G[k].shape for k in small])
    reduced = {}
    for k, gsum in zip(small, summed):
        if k in ("conv_dw_w", "ffn_conv_w"):
            cols = weights[k].shape[2]
            reduced[k] = lax.dynamic_slice_in_dim(gsum, chip * cols, cols, axis=1)
        else:
            reduced[k] = gsum

    grads, deltas, new_m, new_v = {}, {}, {}, {}

    def update(keys):
        for k in keys:
            d, mn, vn = _adamw_call(weights[k][0], reduced[k], m_in[k][0], v_in[k][0], f"adamw_{k}")
            grads[k], deltas[k], new_m[k], new_v[k] = reduced[k][None], d[None], mn[None], vn[None]

    others = [k for k in big if k != "w_in"]
    reduced.update(finish(others, summed_block, "others"))
    update(others)
    reduced.update(finish(["w_in"], deltas["w_up"], "w_in"))
    update(["w_in"])
    flat2 = lambda t: t.reshape(-1, t.shape[-1]) if t.ndim == 3 else t
    pw = _pack([flat2(weights[k]) for k in small])
    pg = _pack([reduced[k] for k in small])
    pm = _pack([flat2(m_in[k]) for k in small])
    pv = _pack([flat2(v_in[k]) for k in small])
    pd, pmn, pvn = _adamw_call(pw, pg, pm, pv, "adamw_small")
    shapes = [weights[k].shape for k in small]
    for k, gk, dk_, mk, vk in zip(small, [reduced[k] for k in small], _unpack(pd, shapes), _unpack(pmn, shapes), _unpack(pvn, shapes)):
        grads[k], deltas[k], new_m[k], new_v[k] = gk.reshape(weights[k].shape), dk_, mk, vk

    return (loss, grad_x[None], *[grads[k] for k in names], *[deltas[k] for k in names],
            *[new_m[k] for k in names], *[new_v[k] for k in names])
```

```python
import functools
import math

import jax
import jax.numpy as jnp
import numpy as np
from jax import lax
from jax.experimental import pallas as pl
from jax.experimental.pallas import tpu as pltpu

F32 = jnp.float32
BF16 = jnp.bfloat16
MESH = pl.DeviceIdType.MESH

HEAD_DIM = 128
HEADS_PER_GROUP = 4
DILATED_PATTERNS = ((128, 1), (512, 4), (2048, 16))
N_GROUPS = 3
N_HEADS = N_GROUPS * HEADS_PER_GROUP
SPAN = 128
GROUP_WIDTH = HEADS_PER_GROUP * HEAD_DIM
CONV_WIDTH = 31
FFN_CONV_WIDTH = 3
N_BUCKETS = 32
MAX_DISTANCE = 2048
RMS_EPS = 1e-6
LN_EPS = 1e-5
NEG_INF = -1e30
ADAM_LR = 0.001
ADAM_B1 = 0.9
ADAM_B2 = 0.999
ADAM_EPS = 1e-08
ADAM_WD = 0.01
ADAM_STEP = 10

LANES = 128
SUBLANES = 8
ROW_TILE = 256
TIME_BLOCK = 128
CONV_PAD = 32
FFN_PAD = 8
VMEM_LIMIT = 56 << 20


def _params(sem=None, vmem=None):
    kw = {}
    if sem is not None:
        kw["dimension_semantics"] = sem
    if vmem is not None:
        kw["vmem_limit_bytes"] = vmem
    return pltpu.CompilerParams(**kw)


def _pick(n, cands):
    for c in cands:
        if n % c == 0:
            return c
    return n


ELEMENTWISE_TILE_BYTES = 1 << 20


def _row_tile(rows, cols):
    for c in (512, 256, 128, 64, 32, 16):
        if rows % c == 0 and c * cols * 4 <= ELEMENTWISE_TILE_BYTES:
            return c
    return 16 if rows % 16 == 0 else 8


N_CHIPS = 4
M_TILES = (1024, 1408, 512, 256, 128)
N_TILES = (512, 1408, 256, 128)
K_TILES = (2176, 2048, 1408, 1024, 512, 256, 128)


def _matmul(a, b, mode, name, out_shards=False, tm=None):
    assert a.dtype == BF16 and b.dtype == BF16, (name, a.dtype, b.dtype)
    b3 = b.ndim == 3
    tn = tk = None
    if mode == "nn":
        M, K = a.shape
        N = b.shape[-1] * (N_CHIPS if b3 else 1)
        tn = b.shape[-1] if b3 else None
    elif mode == "nt":
        M, K = a.shape
        N = b.shape[-2]
        tk = b.shape[-1] if b3 else None
    else:
        K, M = a.shape
        N = b.shape[1]
        tn = N // N_CHIPS if out_shards else None
    tm = tm or _pick(M, M_TILES)
    tn = tn or _pick(N, N_TILES)
    tk = tk or _pick(K, K_TILES)
    nk = K // tk
    dn = {"nn": (((1,), (0,)), ((), ())), "nt": (((1,), (1,)), ((), ())), "tn": (((0,), (0,)), ((), ()))}[mode]

    def body(a_ref, b_ref, o_ref):
        if nk == 1:
            o_ref[...] = lax.dot_general(a_ref[...], b_ref[...], dn, preferred_element_type=F32)
        else:
            @pl.when(pl.program_id(2) == 0)
            def _():
                o_ref[...] = jnp.zeros_like(o_ref)

            o_ref[...] += lax.dot_general(a_ref[...], b_ref[...], dn, preferred_element_type=F32)

    if mode == "tn":
        a_spec = pl.BlockSpec((tk, tm), lambda i, j, k: (k, i))
    else:
        a_spec = pl.BlockSpec((tm, tk), lambda i, j, k: (i, k))
    if mode == "nn":
        b_spec = pl.BlockSpec((None, tk, tn), lambda i, j, k: (j, k, 0)) if b3 else pl.BlockSpec((tk, tn), lambda i, j, k: (k, j))
    elif mode == "nt":
        b_spec = pl.BlockSpec((None, tn, tk), lambda i, j, k: (k, j, 0)) if b3 else pl.BlockSpec((tn, tk), lambda i, j, k: (j, k))
    else:
        b_spec = pl.BlockSpec((tk, tn), lambda i, j, k: (k, j))
    if out_shards:
        out_spec = pl.BlockSpec((None, tm, tn), lambda i, j, k: (j, i, 0))
        out_shape = jax.ShapeDtypeStruct((N_CHIPS, M, tn), F32)
    else:
        out_spec = pl.BlockSpec((tm, tn), lambda i, j, k: (i, j))
        out_shape = jax.ShapeDtypeStruct((M, N), F32)
    return pl.pallas_call(
        body, name=name, grid=(M // tm, N // tn, nk),
        in_specs=[a_spec, b_spec], out_specs=out_spec, out_shape=out_shape,
        compiler_params=_params(("parallel", "parallel", "arbitrary"), VMEM_LIMIT),
    )(a, b)


def _rms(x, g):
    r = lax.rsqrt(jnp.mean(x * x, axis=-1, keepdims=True) + RMS_EPS)
    return x * r * g


def _rms_bwd(x, g, dy):
    r = lax.rsqrt(jnp.mean(x * x, axis=-1, keepdims=True) + RMS_EPS)
    n = x * r
    dn = dy * g
    dx = r * (dn - n * jnp.mean(dn * n, axis=-1, keepdims=True))
    return dx, jnp.sum(dy * n, axis=0, keepdims=True)


def _sigmoid(x):
    return 1.0 / (1.0 + jnp.exp(-x))


_GELU_C = math.sqrt(2.0 / math.pi)


def _gelu(x):
    return 0.5 * x * (1.0 + jnp.tanh(_GELU_C * (x + 0.044715 * x * x * x)))


def _gelu_grad(x):
    t = jnp.tanh(_GELU_C * (x + 0.044715 * x * x * x))
    return 0.5 * (1.0 + t) + 0.5 * x * (1.0 - t * t) * _GELU_C * (1.0 + 3.0 * 0.044715 * x * x)


def _row_spec(width, col_block=0):
    return pl.BlockSpec((ROW_TILE, width), lambda i: (i, col_block))


def _vec_spec(width, col_block=0):
    return pl.BlockSpec((1, width), lambda i: (0, col_block))


def _accumulate(ref, part):
    @pl.when(pl.program_id(0) == 0)
    def _():
        ref[...] = part

    @pl.when(pl.program_id(0) > 0)
    def _():
        ref[...] += part


def _rms_fwd_call(x, g):
    S, D = x.shape

    def body(x_ref, g_ref, h_ref):
        h_ref[...] = _rms(x_ref[...], g_ref[...]).astype(BF16)

    return pl.pallas_call(
        body, name="rms_mix_pre", grid=(S // ROW_TILE,),
        in_specs=[_row_spec(D), _vec_spec(D)], out_specs=_row_spec(D),
        out_shape=jax.ShapeDtypeStruct((S, D), BF16),
        compiler_params=_params(("parallel",)),
    )(x, g)


def _ln_silu_call(c1, g, b):
    S, C = c1.shape

    def body(c_ref, g_ref, b_ref, o_ref):
        xv = c_ref[...]
        mu = jnp.mean(xv, axis=-1, keepdims=True)
        xc = xv - mu
        var = jnp.mean(xc * xc, axis=-1, keepdims=True)
        z = xc * lax.rsqrt(var + LN_EPS) * g_ref[...] + b_ref[...]
        o_ref[...] = (z * _sigmoid(z)).astype(BF16)

    return pl.pallas_call(
        body, name="conv_ln_silu", grid=(S // ROW_TILE,),
        in_specs=[_row_spec(C), _vec_spec(C), _vec_spec(C)], out_specs=_row_spec(C),
        out_shape=jax.ShapeDtypeStruct((S, C), BF16),
        compiler_params=_params(("parallel",)),
    )(c1, g, b)


def _ln_silu_bwd_call(c1, g, b, dc):
    S, C = c1.shape

    def body(c_ref, g_ref, b_ref, dc_ref, dx_ref, dg_ref, db_ref):
        xv = c_ref[...]
        mu = jnp.mean(xv, axis=-1, keepdims=True)
        xc = xv - mu
        rs = lax.rsqrt(jnp.mean(xc * xc, axis=-1, keepdims=True) + LN_EPS)
        xh = xc * rs
        z = xh * g_ref[...] + b_ref[...]
        sg = _sigmoid(z)
        dz = dc_ref[...] * (sg * (1.0 + z * (1.0 - sg)))
        dxh = dz * g_ref[...]
        dx_ref[...] = rs * (dxh - jnp.mean(dxh, axis=-1, keepdims=True) - xh * jnp.mean(dxh * xh, axis=-1, keepdims=True))
        _accumulate(dg_ref, jnp.sum(dz * xh, axis=0, keepdims=True))
        _accumulate(db_ref, jnp.sum(dz, axis=0, keepdims=True))

    return pl.pallas_call(
        body, name="conv_ln_silu_bwd", grid=(S // ROW_TILE,),
        in_specs=[_row_spec(C), _vec_spec(C), _vec_spec(C), _row_spec(C)],
        out_specs=[_row_spec(C), _vec_spec(C), _vec_spec(C)],
        out_shape=[jax.ShapeDtypeStruct((S, C), F32), jax.ShapeDtypeStruct((1, C), F32), jax.ShapeDtypeStruct((1, C), F32)],
        compiler_params=_params(("arbitrary",)),
    )(c1, g, b, dc)


def _mix_call(proj, gate_col0, b_gate, y_a, y_c):
    S, D = y_a.shape
    w = 512
    nc = D // w
    ga0, gc0 = gate_col0 // w, (gate_col0 + D) // w

    def body(ga_ref, gc_ref, ba_ref, bc_ref, ya_ref, yc_ref, o_ref):
        o_ref[...] = (_sigmoid(ga_ref[...] + ba_ref[...]) * ya_ref[...]
                      + _sigmoid(gc_ref[...] + bc_ref[...]) * yc_ref[...]).astype(BF16)

    tile = lambda off: pl.BlockSpec((ROW_TILE, w), lambda i, j: (i, off + j))
    vec = lambda off: pl.BlockSpec((1, w), lambda i, j: (0, off + j))
    return pl.pallas_call(
        body, name="gate_mix", grid=(S // ROW_TILE, nc),
        in_specs=[tile(ga0), tile(gc0), vec(0), vec(nc), tile(0), tile(0)],
        out_specs=tile(0), out_shape=jax.ShapeDtypeStruct((S, D), BF16),
        compiler_params=_params(("parallel", "parallel")),
    )(proj, proj, b_gate, b_gate, y_a, y_c)


def _mix_bwd_call(dmixed, proj, gate_col0, b_gate, y_a, y_c):
    S, D = y_a.shape
    w = 512
    nc = D // w
    ga0, gc0 = gate_col0 // w, (gate_col0 + D) // w

    def body(dm_ref, ga_ref, gc_ref, ba_ref, bc_ref, ya_ref, yc_ref, dya_ref, dyc_ref, dga_ref, dgc_ref, dba_ref, dbc_ref):
        dm = dm_ref[...]
        sa = _sigmoid(ga_ref[...] + ba_ref[...])
        sc = _sigmoid(gc_ref[...] + bc_ref[...])
        dya_ref[...] = (dm * sa).astype(BF16)
        dyc_ref[...] = (dm * sc).astype(BF16)
        dga = dm * ya_ref[...] * sa * (1.0 - sa)
        dgc = dm * yc_ref[...] * sc * (1.0 - sc)
        dga_ref[...] = dga.astype(BF16)
        dgc_ref[...] = dgc.astype(BF16)
        pa = jnp.sum(dga, axis=0, keepdims=True)
        pc = jnp.sum(dgc, axis=0, keepdims=True)

        @pl.when(pl.program_id(1) == 0)
        def _():
            dba_ref[...] = pa
            dbc_ref[...] = pc

        @pl.when(pl.program_id(1) > 0)
        def _():
            dba_ref[...] += pa
            dbc_ref[...] += pc

    tile = lambda off: pl.BlockSpec((ROW_TILE, w), lambda j, i: (i, off + j))
    vec = lambda off: pl.BlockSpec((1, w), lambda j, i: (0, off + j))
    return pl.pallas_call(
        body, name="gate_mix_bwd", grid=(nc, S // ROW_TILE),
        in_specs=[tile(0), tile(ga0), tile(gc0), vec(0), vec(nc), tile(0), tile(0)],
        out_specs=[tile(0), tile(0), tile(0), tile(0), vec(0), vec(0)],
        out_shape=[jax.ShapeDtypeStruct((S, D), BF16)] * 4 + [
                   jax.ShapeDtypeStruct((1, D), F32), jax.ShapeDtypeStruct((1, D), F32)],
        compiler_params=_params(("parallel", "arbitrary")),
    )(dmixed, proj, proj, b_gate, b_gate, y_a, y_c)


def _res1_call(x, out, g_post, g_pre):
    S, D = x.shape

    def body(x_ref, o_ref, gp_ref, gq_ref, x1_ref, h2_ref):
        x1 = x_ref[...] + _rms(o_ref[...], gp_ref[...])
        x1_ref[...] = x1
        h2_ref[...] = _rms(x1, gq_ref[...]).astype(BF16)

    return pl.pallas_call(
        body, name="residual_mix", grid=(S // ROW_TILE,),
        in_specs=[_row_spec(D), _row_spec(D), _vec_spec(D), _vec_spec(D)],
        out_specs=[_row_spec(D), _row_spec(D)],
        out_shape=[jax.ShapeDtypeStruct((S, D), F32), jax.ShapeDtypeStruct((S, D), BF16)],
        compiler_params=_params(("parallel",)),
    )(x, out, g_post, g_pre)


def _loss_call(y, x1, g_post, target):
    S, D = y.shape

    def body(y_ref, x1_ref, g_ref, t_ref, loss_ref, dx_ref):
        err = x1_ref[...] + _rms(y_ref[...], g_ref[...]) - t_ref[...]
        dx_ref[...] = err * (1.0 / D)
        part = 0.5 * jnp.sum(jnp.mean(err * err, axis=-1, keepdims=True), axis=0, keepdims=True)
        _accumulate(loss_ref, jnp.broadcast_to(part, (SUBLANES, LANES)))

    return pl.pallas_call(
        body, name="residual_ffn_loss", grid=(S // ROW_TILE,),
        in_specs=[_row_spec(D), _row_spec(D), _vec_spec(D), _row_spec(D)],
        out_specs=[pl.BlockSpec((SUBLANES, LANES), lambda i: (0, 0)), _row_spec(D)],
        out_shape=[jax.ShapeDtypeStruct((SUBLANES, LANES), F32), jax.ShapeDtypeStruct((S, D), F32)],
        compiler_params=_params(("arbitrary",)),
    )(y, x1, g_post, target)


def _rms_bwd_call(x, g, dy, name):
    S, D = x.shape

    def body(x_ref, g_ref, dy_ref, dx_ref, dg_ref):
        dx, dg = _rms_bwd(x_ref[...], g_ref[...], dy_ref[...])
        dx_ref[...] = dx.astype(BF16)
        _accumulate(dg_ref, dg)

    return pl.pallas_call(
        body, name=name, grid=(S // ROW_TILE,),
        in_specs=[_row_spec(D), _vec_spec(D), _row_spec(D)],
        out_specs=[_row_spec(D), _vec_spec(D)],
        out_shape=[jax.ShapeDtypeStruct((S, D), BF16), jax.ShapeDtypeStruct((1, D), F32)],
        compiler_params=_params(("arbitrary",)),
    )(x, g, dy)


def _mid_bwd_call(x1, g_pre, dh2, dx2, out, g_post):
    S, D = x1.shape

    def body(x1_ref, gq_ref, dh_ref, dx2_ref, o_ref, gp_ref, dx1_ref, do_ref, dgq_ref, dgp_ref):
        d, dgq = _rms_bwd(x1_ref[...], gq_ref[...], dh_ref[...])
        dx1 = dx2_ref[...] + d
        dx1_ref[...] = dx1
        do, dgp = _rms_bwd(o_ref[...], gp_ref[...], dx1)
        do_ref[...] = do.astype(BF16)
        _accumulate(dgq_ref, dgq)
        _accumulate(dgp_ref, dgp)

    return pl.pallas_call(
        body, name="residual_mix_bwd", grid=(S // ROW_TILE,),
        in_specs=[_row_spec(D), _vec_spec(D), _row_spec(D), _row_spec(D), _row_spec(D), _vec_spec(D)],
        out_specs=[_row_spec(D), _row_spec(D), _vec_spec(D), _vec_spec(D)],
        out_shape=[jax.ShapeDtypeStruct((S, D), F32), jax.ShapeDtypeStruct((S, D), BF16)] + [jax.ShapeDtypeStruct((1, D), F32)] * 2,
        compiler_params=_params(("arbitrary",)),
    )(x1, g_pre, dh2, dx2, out, g_post)


def _in_bwd_call(x, g, dh1, dx1):
    S, D = x.shape

    def body(x_ref, g_ref, dh_ref, dx1_ref, gx_ref, dg_ref):
        d, dg = _rms_bwd(x_ref[...], g_ref[...], dh_ref[...])
        gx_ref[...] = dx1_ref[...] + d
        _accumulate(dg_ref, dg)

    return pl.pallas_call(
        body, name="rms_mix_pre_bwd", grid=(S // ROW_TILE,),
        in_specs=[_row_spec(D), _vec_spec(D), _row_spec(D), _row_spec(D)],
        out_specs=[_row_spec(D), _vec_spec(D)],
        out_shape=[jax.ShapeDtypeStruct((S, D), F32), jax.ShapeDtypeStruct((1, D), F32)],
        compiler_params=_params(("arbitrary",)),
    )(x, g, dh1, dx1)


def _bucket_table(dilation):
    qi = np.arange(SPAN)[:, None]
    ki = np.arange(2 * SPAN)[None, :]
    dist = np.maximum(qi + SPAN - ki, 0) * dilation
    max_exact = N_BUCKETS // 2
    d = np.maximum(dist, 1).astype(np.float64)
    large = max_exact + (np.log(d / max_exact) / math.log(MAX_DISTANCE / max_exact) * (N_BUCKETS - max_exact)).astype(np.int32)
    large = np.minimum(large, N_BUCKETS - 1)
    return np.where(dist < max_exact, dist, large).astype(np.int32)


def _bucket_tables():
    return jnp.asarray(np.stack([_bucket_table(r) for _, r in DILATED_PATTERNS]))


def _bias_table_call(rel_bias, buckets):
    def body(rb_ref, bk_ref, o_ref):
        for h in range(N_HEADS):
            bk = bk_ref[h // HEADS_PER_GROUP]

            def step(b, acc):
                return jnp.where(bk == b, rb_ref[b, h], acc)

            o_ref[h] = lax.fori_loop(0, N_BUCKETS, step, jnp.zeros((SPAN, 2 * SPAN), F32))

    return pl.pallas_call(
        body, name="rel_bias_table",
        in_specs=[pl.BlockSpec(memory_space=pltpu.SMEM), pl.BlockSpec(memory_space=pltpu.VMEM)],
        out_specs=pl.BlockSpec(memory_space=pltpu.VMEM),
        out_shape=jax.ShapeDtypeStruct((N_HEADS, SPAN, 2 * SPAN), F32),
    )(rel_bias, buckets)


def _bias_grad_call(dbias, buckets):
    def body(db_ref, bk_ref, o_ref, rows_ref):
        for h in range(N_HEADS):
            bk = bk_ref[h // HEADS_PER_GROUP]
            dv = db_ref[h]

            def step(b, carry):
                rows_ref[h, b] = jnp.sum(jnp.where(bk == b, dv, 0.0), axis=0, keepdims=True)
                return carry

            lax.fori_loop(0, N_BUCKETS, step, 0)
        o_ref[...] = jnp.sum(rows_ref[...], axis=-1, keepdims=True)

    out = pl.pallas_call(
        body, name="rel_bias_grad",
        in_specs=[pl.BlockSpec(memory_space=pltpu.VMEM), pl.BlockSpec(memory_space=pltpu.VMEM)],
        out_specs=pl.BlockSpec(memory_space=pltpu.VMEM),
        out_shape=jax.ShapeDtypeStruct((N_HEADS, N_BUCKETS, 1, 1), F32),
        scratch_shapes=[pltpu.VMEM((N_HEADS, N_BUCKETS, 1, 2 * SPAN), F32)],
    )(dbias, buckets)
    return out.reshape(N_HEADS, N_BUCKETS).T


def _dot_nt(a, b):
    return lax.dot_general(a, b, (((1,), (1,)), ((), ())), preferred_element_type=F32)


def _dot_nn(a, b):
    return lax.dot_general(a, b, (((1,), (0,)), ((), ())), preferred_element_type=F32)


def _dot_tn(a, b):
    return lax.dot_general(a, b, (((0,), (0,)), ((), ())), preferred_element_type=F32)


def _band_masks(n, nb):
    qi = lax.broadcasted_iota(jnp.int32, (SPAN, SPAN), 0)
    ki = lax.broadcasted_iota(jnp.int32, (SPAN, SPAN), 1)
    prev_ok = jnp.logical_and(ki >= qi, n > 0)
    cur_ok = ki <= qi
    next_ok = jnp.logical_and(ki >= qi, n < nb - 1)
    return prev_ok, cur_ok, next_ok


def _regroup_call(src, col_block, r, inverse, name):
    S = src.shape[0]
    L = S // r
    nt = GROUP_WIDTH // LANES

    def body(x_ref, o_ref):
        def step(rho, carry):
            rows = pl.ds(pl.multiple_of(rho * L, L), L)
            if inverse:
                o_ref[pl.ds(rho, L, stride=r), :] = x_ref[rows, :]
            else:
                o_ref[rows, :] = x_ref[pl.ds(rho, L, stride=r), :]
            return carry

        lax.fori_loop(0, r, step, 0)

    return pl.pallas_call(
        body, name=name, grid=(nt,),
        in_specs=[pl.BlockSpec((S, LANES), lambda i: (0, col_block * nt + i))],
        out_specs=pl.BlockSpec((S, LANES), lambda i: (0, i)),
        out_shape=jax.ShapeDtypeStruct((S, GROUP_WIDTH), F32),
        compiler_params=_params(("parallel",)),
    )(src)


def _to_group_order(src, col_block, group, name):
    r = DILATED_PATTERNS[group][1]
    if r == 1:
        return src, col_block
    return _regroup_call(src, col_block, r, False, name), 0


def _to_token_order(arr, group, name):
    r = DILATED_PATTERNS[group][1]
    return arr if r == 1 else _regroup_call(arr, 0, r, True, name)


def _attn_fwd_call(q, k, v, bias, group):
    S = q[0].shape[0]
    r = DILATED_PATTERNS[group][1]
    nb = S // r // SPAN
    scale = HEAD_DIM ** -0.5

    def body(q_ref, kp_ref, kc_ref, vp_ref, vc_ref, b_ref, o_ref, lse_ref):
        n = pl.program_id(1)
        prev_ok, cur_ok, _ = _band_masks(n, nb)
        for j in range(HEADS_PER_GROUP):
            sl = slice(j * HEAD_DIM, (j + 1) * HEAD_DIM)
            q = q_ref[:, sl].astype(BF16)
            sp = _dot_nt(q, kp_ref[:, sl].astype(BF16)) * scale + b_ref[j, :, :SPAN]
            sc = _dot_nt(q, kc_ref[:, sl].astype(BF16)) * scale + b_ref[j, :, SPAN:]
            sp = jnp.where(prev_ok, sp, NEG_INF)
            sc = jnp.where(cur_ok, sc, NEG_INF)
            m = jnp.maximum(jnp.max(sp, axis=-1, keepdims=True), jnp.max(sc, axis=-1, keepdims=True))
            pp = jnp.exp(sp - m)
            pc = jnp.exp(sc - m)
            den = jnp.sum(pp, axis=-1, keepdims=True) + jnp.sum(pc, axis=-1, keepdims=True)
            o_ref[:, sl] = (_dot_nn(pp.astype(BF16), vp_ref[:, sl].astype(BF16))
                            + _dot_nn(pc.astype(BF16), vc_ref[:, sl].astype(BF16))) / den
            lse_ref[:, sl] = jnp.broadcast_to(m + jnp.log(den), (SPAN, HEAD_DIM))

    blk = (SPAN, GROUP_WIDTH)
    cur = lambda cb: pl.BlockSpec(blk, lambda rho, n: (rho * nb + n, cb))
    prev = lambda cb: pl.BlockSpec(blk, lambda rho, n: (rho * nb + jnp.maximum(n - 1, 0), cb))
    return pl.pallas_call(
        body, name=f"attn_fwd_g{group}", grid=(r, nb),
        in_specs=[cur(q[1]), prev(k[1]), cur(k[1]), prev(v[1]), cur(v[1]),
                  pl.BlockSpec((HEADS_PER_GROUP, SPAN, 2 * SPAN), lambda rho, n: (group, 0, 0))],
        out_specs=[cur(0)] * 2,
        out_shape=[jax.ShapeDtypeStruct((S, GROUP_WIDTH), F32)] * 2,
        compiler_params=_params(("parallel", "parallel")),
    )(q[0], k[0], k[0], v[0], v[0], bias)


def _attn_merge_call(parts):
    S = parts[0].shape[0]

    def body(o1, s1, o2, s2, o3, s3, a_ref, ab_ref, lse_ref):
        mx = jnp.maximum(jnp.maximum(s1[...], s2[...]), s3[...])
        w1 = jnp.exp(s1[...] - mx)
        w2 = jnp.exp(s2[...] - mx)
        w3 = jnp.exp(s3[...] - mx)
        den = w1 + w2 + w3
        a = (w1 * o1[...] + w2 * o2[...] + w3 * o3[...]) / den
        a_ref[...] = a
        ab_ref[...] = a.astype(BF16)
        lse_ref[...] = mx + jnp.log(den)

    return pl.pallas_call(
        body, name="attn_merge", grid=(S // ROW_TILE,),
        in_specs=[_row_spec(GROUP_WIDTH)] * 6, out_specs=[_row_spec(GROUP_WIDTH)] * 3,
        out_shape=[jax.ShapeDtypeStruct((S, GROUP_WIDTH), F32), jax.ShapeDtypeStruct((S, GROUP_WIDTH), BF16),
                   jax.ShapeDtypeStruct((S, GROUP_WIDTH), F32)],
        compiler_params=_params(("parallel",)),
    )(*parts)


def _attn_delta_call(a, da):
    S = a.shape[0]

    def body(a_ref, da_ref, d_ref):
        for j in range(HEADS_PER_GROUP):
            sl = slice(j * HEAD_DIM, (j + 1) * HEAD_DIM)
            d = jnp.sum(a_ref[:, sl] * da_ref[:, sl], axis=-1, keepdims=True)
            d_ref[:, sl] = jnp.broadcast_to(d, (ROW_TILE, HEAD_DIM))

    return pl.pallas_call(
        body, name="attn_delta", grid=(S // ROW_TILE,),
        in_specs=[_row_spec(GROUP_WIDTH)] * 2, out_specs=_row_spec(GROUP_WIDTH),
        out_shape=jax.ShapeDtypeStruct((S, GROUP_WIDTH), F32),
        compiler_params=_params(("parallel",)),
    )(a, da)


def _attn_bwd_call(q, k, v, bias, da, lse, delta, group):
    S = q[0].shape[0]
    r = DILATED_PATTERNS[group][1]
    nb = S // r // SPAN
    scale = HEAD_DIM ** -0.5

    def body(q_ref, qn_ref, kp_ref, kc_ref, vp_ref, vc_ref, b_ref, da_ref, dan_ref, lse_ref, lsen_ref, dl_ref, dln_ref,
             dq_ref, dk_ref, dv_ref, db_ref):
        n = pl.program_id(1)
        prev_ok, cur_ok, next_ok = _band_masks(n, nb)
        first = jnp.logical_and(pl.program_id(0) == 0, n == 0)
        for j in range(HEADS_PER_GROUP):
            sl = slice(j * HEAD_DIM, (j + 1) * HEAD_DIM)
            q = q_ref[:, sl].astype(BF16)
            qn = qn_ref[:, sl].astype(BF16)
            kp = kp_ref[:, sl].astype(BF16)
            kc = kc_ref[:, sl].astype(BF16)
            vp = vp_ref[:, sl].astype(BF16)
            vc = vc_ref[:, sl].astype(BF16)
            dav = da_ref[:, sl].astype(BF16)
            dan = dan_ref[:, sl].astype(BF16)
            bp = b_ref[j, :, :SPAN]
            bc = b_ref[j, :, SPAN:]
            pp = jnp.exp(jnp.where(prev_ok, _dot_nt(q, kp) * scale + bp, NEG_INF) - lse_ref[:, sl])
            pc = jnp.exp(jnp.where(cur_ok, _dot_nt(q, kc) * scale + bc, NEG_INF) - lse_ref[:, sl])
            pn = jnp.exp(jnp.where(next_ok, _dot_nt(qn, kc) * scale + bp, NEG_INF) - lsen_ref[:, sl])
            dsp = pp * (_dot_nt(dav, vp) - dl_ref[:, sl])
            dsc = pc * (_dot_nt(dav, vc) - dl_ref[:, sl])
            dsn = pn * (_dot_nt(dan, vc) - dln_ref[:, sl])
            dsp_b, dsc_b, dsn_b = dsp.astype(BF16), dsc.astype(BF16), dsn.astype(BF16)
            dq_ref[:, sl] = (_dot_nn(dsp_b, kp) + _dot_nn(dsc_b, kc)) * scale
            dk_ref[:, sl] = (_dot_tn(dsc_b, q) + _dot_tn(dsn_b, qn)) * scale
            dv_ref[:, sl] = _dot_tn(pc.astype(BF16), dav) + _dot_tn(pn.astype(BF16), dan)

            @pl.when(first)
            def _():
                db_ref[j, :, :SPAN] = dsp
                db_ref[j, :, SPAN:] = dsc

            @pl.when(jnp.logical_not(first))
            def _():
                db_ref[j, :, :SPAN] += dsp
                db_ref[j, :, SPAN:] += dsc

    blk = (SPAN, GROUP_WIDTH)
    cur = lambda cb: pl.BlockSpec(blk, lambda rho, n: (rho * nb + n, cb))
    prev = lambda cb: pl.BlockSpec(blk, lambda rho, n: (rho * nb + jnp.maximum(n - 1, 0), cb))
    nxt = lambda cb: pl.BlockSpec(blk, lambda rho, n: (rho * nb + jnp.minimum(n + 1, nb - 1), cb))
    band = (HEADS_PER_GROUP, SPAN, 2 * SPAN)
    return pl.pallas_call(
        body, name=f"attn_bwd_g{group}", grid=(r, nb),
        in_specs=[cur(q[1]), nxt(q[1]), prev(k[1]), cur(k[1]), prev(v[1]), cur(v[1]),
                  pl.BlockSpec(band, lambda rho, n: (group, 0, 0)),
                  cur(0), nxt(0), cur(0), nxt(0), cur(0), nxt(0)],
        out_specs=[cur(0), cur(0), cur(0), pl.BlockSpec(band, lambda rho, n: (0, 0, 0))],
        out_shape=[jax.ShapeDtypeStruct((S, GROUP_WIDTH), F32)] * 3 + [jax.ShapeDtypeStruct(band, F32)],
        compiler_params=_params(("arbitrary", "arbitrary")),
    )(q[0], q[0], k[0], k[0], v[0], v[0], bias, da, da, lse, lse, delta, delta)


def _taps(width):
    return [(k, (width - 1 - k) // SUBLANES, (width - 1 - k) % SUBLANES) for k in range(width)]


def _shifted(win, width, pad, up):
    total = win.shape[0]
    for b in range(SUBLANES):
        taps = [(k, a) for k, a, bb in _taps(width) if bb == b]
        if not taps:
            continue
        if up:
            rolled = win if b == 0 else pltpu.roll(win, total - b, axis=0)
        else:
            rolled = win if b == 0 else pltpu.roll(win, b, axis=0)
        for k, a in taps:
            start = SUBLANES * a if up else pad - SUBLANES * a
            yield k, rolled[start:start + TIME_BLOCK, :]


def _conv_block(win, w_ref, width, pad):
    acc = None
    for k, rows in _shifted(win, width, pad, up=False):
        term = w_ref[k:k + 1, :] * rows
        acc = term if acc is None else acc + term
    return acc


def _conv_transpose_block(win, w_ref, width, pad):
    acc = None
    for k, rows in _shifted(win, width, pad, up=True):
        term = w_ref[k:k + 1, :] * rows
        acc = term if acc is None else acc + term
    return acc


def _conv_weight_grad(win, dy, dw_ref, width, pad):
    for k, rows in _shifted(win, width, pad, up=False):
        dw_ref[k:k + 1, :] += jnp.sum(dy * rows, axis=0, keepdims=True)


def _time_loop(S, step):
    def it(tb, carry):
        step(pl.multiple_of(tb * TIME_BLOCK, TIME_BLOCK))
        return carry

    lax.fori_loop(0, S // TIME_BLOCK, it, 0)


def _conv_fwd_call(proj, col0, w, b):
    S = proj.shape[0]
    C = w.shape[1]
    nt = C // LANES
    v0, g0 = col0 // LANES, (col0 + C) // LANES

    def body(val_ref, gate_ref, w_ref, b_ref, o_ref, pad_ref):
        pad_ref[0:CONV_PAD, :] = jnp.zeros((CONV_PAD, LANES), F32)
        pad_ref[CONV_PAD:, :] = val_ref[...] * _sigmoid(gate_ref[...])

        def step(t0):
            win = pad_ref[pl.ds(t0, TIME_BLOCK + CONV_PAD), :]
            o_ref[pl.ds(t0, TIME_BLOCK), :] = _conv_block(win, w_ref, CONV_WIDTH, CONV_PAD) + b_ref[...]

        _time_loop(S, step)

    seq = lambda off: pl.BlockSpec((S, LANES), lambda i: (0, off + i))
    return pl.pallas_call(
        body, name="conv_module", grid=(nt,),
        in_specs=[seq(v0), seq(g0), pl.BlockSpec((CONV_WIDTH, LANES), lambda i: (0, i)), pl.BlockSpec((1, LANES), lambda i: (0, i))],
        out_specs=seq(0), out_shape=jax.ShapeDtypeStruct((S, C), F32),
        scratch_shapes=[pltpu.VMEM((S + CONV_PAD, LANES), F32)],
        compiler_params=_params(("parallel",)),
    )(proj, proj, w, b)


def _conv_bwd_call(proj, col0, w, dc1):
    S = proj.shape[0]
    C = w.shape[1]
    nt = C // LANES
    v0, g0 = col0 // LANES, (col0 + C) // LANES

    def body(val_ref, gate_ref, w_ref, dy_ref, dval_ref, dgate_ref, dw_ref, db_ref, xpad_ref, dpad_ref, dwacc_ref):
        xpad_ref[0:CONV_PAD, :] = jnp.zeros((CONV_PAD, LANES), F32)
        xpad_ref[CONV_PAD:, :] = val_ref[...] * _sigmoid(gate_ref[...])
        dpad_ref[0:S, :] = dy_ref[...]
        dpad_ref[S:, :] = jnp.zeros((CONV_PAD, LANES), F32)
        dwacc_ref[...] = jnp.zeros_like(dwacc_ref)

        def step(t0):
            rows = pl.ds(t0, TIME_BLOCK)
            _conv_weight_grad(xpad_ref[pl.ds(t0, TIME_BLOCK + CONV_PAD), :], dy_ref[rows, :], dwacc_ref, CONV_WIDTH, CONV_PAD)
            dc0 = _conv_transpose_block(dpad_ref[pl.ds(t0, TIME_BLOCK + CONV_PAD), :], w_ref, CONV_WIDTH, CONV_PAD)
            sg = _sigmoid(gate_ref[rows, :])
            dval_ref[rows, :] = (dc0 * sg).astype(BF16)
            dgate_ref[rows, :] = (dc0 * val_ref[rows, :] * sg * (1.0 - sg)).astype(BF16)

        _time_loop(S, step)
        dw_ref[...] = dwacc_ref[...]
        db_ref[...] = jnp.sum(dy_ref[...], axis=0, keepdims=True)

    seq = lambda off: pl.BlockSpec((S, LANES), lambda i: (0, off + i))
    return pl.pallas_call(
        body, name="conv_module_bwd", grid=(nt,),
        in_specs=[seq(v0), seq(g0), pl.BlockSpec((CONV_WIDTH, LANES), lambda i: (0, i)), seq(0)],
        out_specs=[seq(0), seq(0), pl.BlockSpec((CONV_PAD, LANES), lambda i: (0, i)), pl.BlockSpec((1, LANES), lambda i: (0, i))],
        out_shape=[jax.ShapeDtypeStruct((S, C), BF16), jax.ShapeDtypeStruct((S, C), BF16),
                   jax.ShapeDtypeStruct((CONV_PAD, C), F32), jax.ShapeDtypeStruct((1, C), F32)],
        scratch_shapes=[pltpu.VMEM((S + CONV_PAD, LANES), F32), pltpu.VMEM((S + CONV_PAD, LANES), F32),
                        pltpu.VMEM((CONV_PAD, LANES), F32)],
        compiler_params=_params(("parallel",)),
    )(proj, proj, w, dc1)


def _ffn_fwd_call(u, w, b):
    S, C2 = u.shape
    C = C2 // 2
    nt = C // LANES

    def body(ug_ref, uv_ref, wg_ref, wv_ref, bg_ref, bv_ref, f_ref, pg_ref, pv_ref):
        zeros = jnp.zeros((FFN_PAD, LANES), F32)
        pg_ref[0:FFN_PAD, :] = zeros
        pv_ref[0:FFN_PAD, :] = zeros
        pg_ref[FFN_PAD:, :] = ug_ref[...]
        pv_ref[FFN_PAD:, :] = uv_ref[...]

        def step(t0):
            win = pl.ds(t0, TIME_BLOCK + FFN_PAD)
            cg = _conv_block(pg_ref[win, :], wg_ref, FFN_CONV_WIDTH, FFN_PAD) + bg_ref[...]
            cv = _conv_block(pv_ref[win, :], wv_ref, FFN_CONV_WIDTH, FFN_PAD) + bv_ref[...]
            f_ref[pl.ds(t0, TIME_BLOCK), :] = (_gelu(cg) * cv).astype(BF16)

        _time_loop(S, step)

    seq = lambda off: pl.BlockSpec((S, LANES), lambda i: (0, off + i))
    wsp = lambda off: pl.BlockSpec((FFN_CONV_WIDTH, LANES), lambda i: (0, off + i))
    bsp = lambda off: pl.BlockSpec((1, LANES), lambda i: (0, off + i))
    return pl.pallas_call(
        body, name="ffn_conv_geglu", grid=(nt,),
        in_specs=[seq(0), seq(nt), wsp(0), wsp(nt), bsp(0), bsp(nt)],
        out_specs=seq(0), out_shape=jax.ShapeDtypeStruct((S, C), BF16),
        scratch_shapes=[pltpu.VMEM((S + FFN_PAD, LANES), F32)] * 2,
        compiler_params=_params(("parallel",)),
    )(u, u, w, w, b, b)


def _ffn_bwd_call(u, w, b, df):
    S, C2 = u.shape
    C = C2 // 2
    nt = C // LANES

    def body(ug_ref, uv_ref, wg_ref, wv_ref, bg_ref, bv_ref, df_ref,
             dug_ref, duv_ref, dwg_ref, dwv_ref, dbg_ref, dbv_ref,
             pg_ref, pv_ref, dg_ref, dv_ref, dwg_acc, dwv_acc, dbg_acc, dbv_acc):
        zeros = jnp.zeros((FFN_PAD, LANES), F32)
        pg_ref[0:FFN_PAD, :] = zeros
        pv_ref[0:FFN_PAD, :] = zeros
        pg_ref[FFN_PAD:, :] = ug_ref[...]
        pv_ref[FFN_PAD:, :] = uv_ref[...]
        dg_ref[S:, :] = zeros
        dv_ref[S:, :] = zeros
        dwg_acc[...] = jnp.zeros_like(dwg_acc)
        dwv_acc[...] = jnp.zeros_like(dwv_acc)
        dbg_acc[...] = jnp.zeros_like(dbg_acc)
        dbv_acc[...] = jnp.zeros_like(dbv_acc)

        def first(t0):
            win = pl.ds(t0, TIME_BLOCK + FFN_PAD)
            rows = pl.ds(t0, TIME_BLOCK)
            xg = pg_ref[win, :]
            xv = pv_ref[win, :]
            cg = _conv_block(xg, wg_ref, FFN_CONV_WIDTH, FFN_PAD) + bg_ref[...]
            cv = _conv_block(xv, wv_ref, FFN_CONV_WIDTH, FFN_PAD) + bv_ref[...]
            dfb = df_ref[rows, :]
            dcg = dfb * cv * _gelu_grad(cg)
            dcv = dfb * _gelu(cg)
            dg_ref[rows, :] = dcg
            dv_ref[rows, :] = dcv
            _conv_weight_grad(xg, dcg, dwg_acc, FFN_CONV_WIDTH, FFN_PAD)
            _conv_weight_grad(xv, dcv, dwv_acc, FFN_CONV_WIDTH, FFN_PAD)
            dbg_acc[...] += jnp.sum(dcg, axis=0, keepdims=True)
            dbv_acc[...] += jnp.sum(dcv, axis=0, keepdims=True)

        def second(t0):
            win = pl.ds(t0, TIME_BLOCK + FFN_PAD)
            rows = pl.ds(t0, TIME_BLOCK)
            dug_ref[rows, :] = _conv_transpose_block(dg_ref[win, :], wg_ref, FFN_CONV_WIDTH, FFN_PAD).astype(BF16)
            duv_ref[rows, :] = _conv_transpose_block(dv_ref[win, :], wv_ref, FFN_CONV_WIDTH, FFN_PAD).astype(BF16)

        _time_loop(S, first)
        _time_loop(S, second)
        dwg_ref[...] = dwg_acc[...]
        dwv_ref[...] = dwv_acc[...]
        dbg_ref[...] = dbg_acc[...]
        dbv_ref[...] = dbv_acc[...]

    seq = lambda off: pl.BlockSpec((S, LANES), lambda i: (0, off + i))
    wsp = lambda off: pl.BlockSpec((FFN_CONV_WIDTH, LANES), lambda i: (0, off + i))
    bsp = lambda off: pl.BlockSpec((1, LANES), lambda i: (0, off + i))
    return pl.pallas_call(
        body, name="ffn_conv_geglu_bwd", grid=(nt,),
        in_specs=[seq(0), seq(nt), wsp(0), wsp(nt), bsp(0), bsp(nt), seq(0)],
        out_specs=[seq(0), seq(0), pl.BlockSpec((SUBLANES, LANES), lambda i: (0, i)), pl.BlockSpec((SUBLANES, LANES), lambda i: (0, i)),
                   bsp(0), bsp(0)],
        out_shape=[jax.ShapeDtypeStruct((S, C), BF16)] * 2 + [jax.ShapeDtypeStruct((SUBLANES, C), F32)] * 2
        + [jax.ShapeDtypeStruct((1, C), F32)] * 2,
        scratch_shapes=[pltpu.VMEM((S + FFN_PAD, LANES), F32)] * 4 + [pltpu.VMEM((SUBLANES, LANES), F32)] * 2
        + [pltpu.VMEM((1, LANES), F32)] * 2,
        compiler_params=_params(("parallel",)),
    )(u, u, w, w, b, b, df)


def _adamw_call(w, g, m, v, name):
    R, C = w.shape
    tr = _row_tile(R, C)
    c1 = 1.0 / (1.0 - ADAM_B1 ** ADAM_STEP)
    c2 = 1.0 / (1.0 - ADAM_B2 ** ADAM_STEP)

    def body(w_ref, g_ref, m_ref, v_ref, d_ref, mo_ref, vo_ref):
        gv = g_ref[...]
        mn = ADAM_B1 * m_ref[...] + (1.0 - ADAM_B1) * gv
        vn = ADAM_B2 * v_ref[...] + (1.0 - ADAM_B2) * (gv * gv)
        mo_ref[...] = mn
        vo_ref[...] = vn
        d_ref[...] = -ADAM_LR * ((mn * c1) / (jnp.sqrt(vn * c2) + ADAM_EPS) + ADAM_WD * w_ref[...])

    spec = pl.BlockSpec((tr, C), lambda i: (i, 0))
    return pl.pallas_call(
        body, name=name, grid=(R // tr,),
        in_specs=[spec] * 4, out_specs=[spec] * 3,
        out_shape=[jax.ShapeDtypeStruct((R, C), F32)] * 3,
        compiler_params=_params(("parallel",)),
    )(w, g, m, v)


def _position():
    return lax.axis_index("x"), lax.axis_index("y"), lax.axis_index("c")


def _chip_peers(x, y):
    return [(x, 1 - y), (1 - x, y), (1 - x, 1 - y)]


def _half_rows(ref, core, rows):
    h = rows // 2
    start = pl.multiple_of(core * h, 16)
    return ref.at[pl.ds(start, h), :] if len(ref.shape) == 2 else ref.at[:, pl.ds(start, h), :]


def _shard_half(ref, shard, core, rows):
    h = rows // 2
    return ref.at[shard, pl.ds(pl.multiple_of(core * h, 16), h), :]


ANY = pl.BlockSpec(memory_space=pl.ANY)


def _allgather_call(shards, whole):
    n, nw = len(shards), len(whole)
    outs_shape = [jax.ShapeDtypeStruct((N_CHIPS,) + s.shape, s.dtype) for s in shards + whole]

    def body(*refs):
        ins, outs = refs[:n + nw], refs[n + nw:2 * (n + nw)]
        send_sems, recv_sems, pass_send, pass_recv, own_send, own_recv = refs[2 * (n + nw):]
        x, y, c = _position()
        chip = 2 * x + y
        peers = _chip_peers(x, y)
        sent, local = [], []
        for i in range(n + nw):
            cp = pltpu.make_async_remote_copy(src_ref=ins[i], dst_ref=outs[i].at[chip], send_sem=own_send.at[i],
                                              recv_sem=own_recv.at[i], device_id=(x, y, 1 - c), device_id_type=MESH)
            cp.start()
            local.append(cp)
            rows = ins[i].shape[0]
            for k, (px, py) in enumerate(peers):
                if i < n:
                    src, dst = _half_rows(ins[i], c, rows), _shard_half(outs[i], chip, c, rows)
                else:
                    src, dst = ins[i], outs[i].at[chip]
                cp = pltpu.make_async_remote_copy(src_ref=src, dst_ref=dst, send_sem=send_sems.at[i, k],
                                                  recv_sem=recv_sems.at[i, k], device_id=(px, py, c), device_id_type=MESH)
                cp.start()
                sent.append(cp)
        passed = []
        for i in range(n + nw):
            rows = ins[i].shape[0]
            for k, (px, py) in enumerate(peers):
                landed = _shard_half(outs[i], 2 * px + py, c, rows) if i < n else outs[i].at[2 * px + py]
                pltpu.make_async_remote_copy(src_ref=landed, dst_ref=landed, send_sem=send_sems.at[i, k],
                                             recv_sem=recv_sems.at[i, k], device_id=(px, py, c), device_id_type=MESH).wait_recv()
                if i < n:
                    cp = pltpu.make_async_remote_copy(src_ref=landed, dst_ref=landed, send_sem=pass_send.at[i, k],
                                                      recv_sem=pass_recv.at[i, k], device_id=(x, y, 1 - c), device_id_type=MESH)
                    cp.start()
                    passed.append(cp)
        for cp in sent:
            cp.wait_send()
        for cp in passed:
            cp.wait()
        for cp in local:
            cp.wait()

    return pl.pallas_call(
        body, name="weight_allgather",
        in_specs=[ANY] * (n + nw), out_specs=[ANY] * (n + nw), out_shape=outs_shape,
        scratch_shapes=[pltpu.SemaphoreType.DMA((n + nw, 3)), pltpu.SemaphoreType.DMA((n + nw, 3)),
                        pltpu.SemaphoreType.DMA((n, 3)), pltpu.SemaphoreType.DMA((n, 3)),
                        pltpu.SemaphoreType.DMA((n + nw,)), pltpu.SemaphoreType.DMA((n + nw,))],
    )(*shards, *whole)


HBM_SPEC = pl.BlockSpec(memory_space=pltpu.HBM)
SEM_SPEC = pl.BlockSpec(memory_space=pltpu.SEMAPHORE)
DATAFLOW = pltpu.SideEffectType.DATAFLOW_SIDE_EFFECTING


def _in_hbm(a):
    return pltpu.with_memory_space_constraint(a, pltpu.HBM)


def _split_start(name, srcs, lands, n_sems, copies, after):
    n, m = len(srcs), len(lands)

    def body(*refs):
        src_refs, land_refs = refs[:n], refs[n:n + m]
        send_sem, recv_sem = refs[n + m + 1], refs[n + m + 2]
        token = refs[-1]
        for src, dst, dev, idx in copies(src_refs, land_refs):
            pltpu.make_async_remote_copy(src_ref=src, dst_ref=dst, send_sem=send_sem.at[idx], recv_sem=recv_sem.at[idx],
                                         device_id=dev, device_id_type=MESH).start()
        token[...] = jnp.zeros_like(token)

    outs = pl.pallas_call(
        body, name=name,
        in_specs=[HBM_SPEC] * (n + m) + [ANY],
        out_specs=[SEM_SPEC, SEM_SPEC] + [HBM_SPEC] * (n + m) + [pl.BlockSpec(memory_space=pltpu.VMEM)],
        out_shape=[pltpu.SemaphoreType.DMA((n_sems,)), pltpu.SemaphoreType.DMA((n_sems,))]
        + [pltpu.HBM(a.shape, a.dtype) for a in list(srcs) + list(lands)] + [jax.ShapeDtypeStruct((SUBLANES, LANES), F32)],
        input_output_aliases={i: 2 + i for i in range(n + m)},
        compiler_params=pltpu.CompilerParams(has_side_effects=DATAFLOW),
    )(*[_in_hbm(a) for a in list(srcs) + list(lands)], after)
    return dict(send=outs[0], recv=outs[1], srcs=list(outs[2:2 + n]), lands=list(outs[2 + n:2 + n + m]),
                tile=outs[-1], token=outs[-1][0, 0])


def _split_wait(name, started, copies, after):
    n, m = len(started["srcs"]), len(started["lands"])

    def body(*refs):
        src_refs, land_refs = refs[:n], refs[n:n + m]
        send_sem, recv_sem = refs[n + m], refs[n + m + 1]
        for src, dst, dev, idx in copies(src_refs, land_refs):
            cp = pltpu.make_async_remote_copy(src_ref=src, dst_ref=dst, send_sem=send_sem.at[idx], recv_sem=recv_sem.at[idx],
                                              device_id=dev, device_id_type=MESH)
            cp.wait_send()
            cp.wait_recv()

    arrays = started["srcs"] + started["lands"]
    outs = pl.pallas_call(
        body, name=name,
        in_specs=[HBM_SPEC] * (n + m) + [SEM_SPEC, SEM_SPEC, ANY],
        out_specs=[HBM_SPEC] * (n + m),
        out_shape=[pltpu.HBM(a.shape, a.dtype) for a in arrays],
        input_output_aliases={i: i for i in range(n + m)},
        compiler_params=pltpu.CompilerParams(has_side_effects=DATAFLOW),
    )(*arrays, started["send"], started["recv"], after)
    return list(outs)


def _gather_copies(srcs, lands):
    x, y, c = _position()
    chip = 2 * x + y
    targets = [(px, py, c) for px, py in _chip_peers(x, y)] + [(x, y, 1 - c)]
    return [(s, l.at[chip], dev, len(targets) * i + k) for i, (s, l) in enumerate(zip(srcs, lands)) for k, dev in enumerate(targets)]


def _sibling_copies(srcs, lands):
    x, y, c = _position()
    return [(_half_rows(srcs[0], 1 - c, srcs[0].shape[1]), lands[0], (x, y, 1 - c), 0)]


def _exchange_copies(srcs, lands):
    x, y, c = _position()
    return [(srcs[0].at[2 * px + py], lands[0].at[k], (px, py, c), k) for k, (px, py) in enumerate(_chip_peers(x, y))]


def _pair_sum_call(grad, recv, core, name):
    _, h, B = recv.shape
    tr = _row_tile(h, B)

    def body(core_ref, g_ref, r_ref, o_ref, ob_ref):
        s = g_ref[...] + r_ref[...]
        o_ref[...] = s
        ob_ref[...] = s.astype(BF16)

    g_spec = pl.BlockSpec((None, tr, B), lambda q, i, core_ref: (q, core_ref[0] * (h // tr) + i, 0))
    spec = pl.BlockSpec((None, tr, B), lambda q, i, core_ref: (q, i, 0))
    return pl.pallas_call(
        body, name=name,
        grid_spec=pltpu.PrefetchScalarGridSpec(num_scalar_prefetch=1, grid=(N_CHIPS, h // tr), in_specs=[g_spec, spec],
                                               out_specs=[spec, spec]),
        out_shape=[jax.ShapeDtypeStruct(recv.shape, F32), jax.ShapeDtypeStruct(recv.shape, BF16)],
        compiler_params=_params(("parallel", "parallel")),
    )(core, grad, recv)


def _chip_sum_call(partial, recv, chip_core, name):
    _, h, B = recv.shape
    tr = _row_tile(h, B)

    def body(cc_ref, p_ref, r_ref, o_ref):
        o_ref[...] = ((p_ref[...] + r_ref[0].astype(F32)) + r_ref[1].astype(F32)) + r_ref[2].astype(F32)

    return pl.pallas_call(
        body, name=name,
        grid_spec=pltpu.PrefetchScalarGridSpec(
            num_scalar_prefetch=1, grid=(h // tr,),
            in_specs=[pl.BlockSpec((None, tr, B), lambda i, cc_ref: (cc_ref[0], i, 0)),
                      pl.BlockSpec((3, tr, B), lambda i, cc_ref: (0, i, 0))],
            out_specs=pl.BlockSpec((tr, B), lambda i, cc_ref: (cc_ref[1] * (h // tr) + i, 0))),
        out_shape=jax.ShapeDtypeStruct((2 * h, B), F32),
        compiler_params=_params(("parallel",)),
    )(chip_core, partial, recv)


def _sibling_assemble_call(shards, name="grad_sibling_assemble"):
    n = len(shards)

    def body(*refs):
        ins, outs = refs[:n], refs[n:2 * n]
        send_sems, recv_sems = refs[2 * n:]
        x, y, c = _position()
        copies = []
        for i in range(n):
            rows = shards[i].shape[0]
            cp = pltpu.make_async_remote_copy(src_ref=_half_rows(ins[i], c, rows), dst_ref=_half_rows(outs[i], c, rows),
                                              send_sem=send_sems.at[i], recv_sem=recv_sems.at[i],
                                              device_id=(x, y, 1 - c), device_id_type=MESH)
            cp.start()
            copies.append(cp)
        for cp in copies:
            cp.wait()

    return pl.pallas_call(
        body, name=name,
        in_specs=[ANY] * n, out_specs=[ANY] * n,
        out_shape=[jax.ShapeDtypeStruct(s.shape, F32) for s in shards],
        input_output_aliases={i: i for i in range(n)},
        scratch_shapes=[pltpu.SemaphoreType.DMA((n,)), pltpu.SemaphoreType.DMA((n,))],
    )(*shards)


def _small_allreduce_call(packed):
    rows = packed.shape[0]

    def body(x_ref, o_ref, buf_ref, send_sems, recv_sems):
        x, y, c = _position()
        me = 4 * x + 2 * y + c
        buf_ref[me] = x_ref[...]
        copies = []
        for k in range(1, 8):
            peer = (1 - x if k & 4 else x, 1 - y if k & 2 else y, 1 - c if k & 1 else c)
            cp = pltpu.make_async_remote_copy(src_ref=buf_ref.at[me], dst_ref=buf_ref.at[me], send_sem=send_sems.at[k - 1],
                                              recv_sem=recv_sems.at[k - 1], device_id=peer, device_id_type=MESH)
            cp.start()
            copies.append(cp)
        for cp in copies:
            cp.wait()
        acc = buf_ref[0]
        for d in range(1, 8):
            acc = acc + buf_ref[d]
        o_ref[...] = acc

    return pl.pallas_call(
        body, name="small_grad_allreduce",
        in_specs=[pl.BlockSpec(memory_space=pltpu.VMEM)], out_specs=pl.BlockSpec(memory_space=pltpu.VMEM),
        out_shape=jax.ShapeDtypeStruct((rows, LANES), F32),
        scratch_shapes=[pltpu.VMEM((8, rows, LANES), F32), pltpu.SemaphoreType.DMA((7,)), pltpu.SemaphoreType.DMA((7,))],
    )(packed)


def _pack(arrays):
    flat = jnp.concatenate([a.reshape(-1).astype(F32) for a in arrays])
    rows = -(-flat.shape[0] // LANES)
    rows = -(-rows // SUBLANES) * SUBLANES
    flat = jnp.pad(flat, (0, rows * LANES - flat.shape[0]))
    return flat.reshape(rows, LANES)


def _unpack(packed, shapes):
    flat = packed.reshape(-1)
    out, off = [], 0
    for shp in shapes:
        size = int(np.prod(shp))
        out.append(flat[off:off + size].reshape(shp))
        off += size
    return out


def _local_step(xs, target, P, late_weights, on_grad):
    S, D = xs.shape
    qkv_width = 3 * N_HEADS * HEAD_DIM
    glu_col0, gate_col0 = qkv_width, qkv_width + 2 * D
    shard_major = lambda g: g.reshape(N_CHIPS, g.shape[0] // N_CHIPS, g.shape[1])

    h1 = _rms_fwd_call(xs, P["norm_mix_pre"])
    proj = _matmul(h1, P["w_in"], "nn", "proj_in")
    buckets = _bucket_tables()
    bias = _bias_table_call(P["rel_bias"], buckets)
    qkv, parts = [], []
    for g in range(N_GROUPS):
        q, k, v = [_to_group_order(proj, 3 * t + g, g, f"group_order_{'qkv'[t]}{g}") for t in range(3)]
        qkv.append((q, k, v))
        o_g, lse_g = _attn_fwd_call(q, k, v, bias, g)
        parts += [_to_token_order(o_g, g, f"token_order_o{g}"), _to_token_order(lse_g, g, f"token_order_lse{g}")]
    a, a_bf, lse = _attn_merge_call(parts)
    P = dict(P, **late_weights("mix", a_bf))
    y_a = _matmul(a_bf, P["w_attn_out"], "nn", "attn_out")
    c1 = _conv_fwd_call(proj, glu_col0, P["conv_dw_w"], P["conv_dw_b"])
    cact = _ln_silu_call(c1, P["conv_ln_g"], P["conv_ln_b"])
    y_c = _matmul(cact, P["conv_pw_w"], "nn", "conv_pw")
    mixed = _mix_call(proj, gate_col0, P["b_gate"], y_a, y_c)
    out = _matmul(mixed, P["w_out"], "nn", "mix_out")
    x1, h2 = _res1_call(xs, out, P["norm_mix_post"], P["norm_ffn_pre"])
    P = dict(P, **late_weights("ffn", h2))
    u = _matmul(h2, P["w_up"], "nn", "ffn_up")
    f = _ffn_fwd_call(u, P["ffn_conv_w"], P["ffn_conv_b"])
    yff = _matmul(f, P["w_down"], "nn", "ffn_down")
    loss_tile, dx2 = _loss_call(yff, x1, P["norm_ffn_post"], target)

    G = {}
    dyff, G["norm_ffn_post"] = _rms_bwd_call(yff, P["norm_ffn_post"], dx2, "rms_ffn_post_bwd")
    zero = on_grad("w_down", shard_major(_matmul(f, dyff, "tn", "ffn_down_dw")))
    df = _matmul(dyff, P["w_down"], "nt", "ffn_down_dx")
    dug, duv, dwg, dwv, dbg, dbv = _ffn_bwd_call(u, P["ffn_conv_w"], P["ffn_conv_b"] + zero, df)
    G["ffn_conv_w"] = jnp.concatenate([dwg[:FFN_CONV_WIDTH], dwv[:FFN_CONV_WIDTH]], axis=1)
    G["ffn_conv_b"] = jnp.concatenate([dbg, dbv], axis=1)
    du = jnp.concatenate([dug, duv], axis=1)
    zero = on_grad("w_up", _matmul(h2, du, "tn", "ffn_up_dw", out_shards=True))
    dh2 = _matmul(du, P["w_up"], "nt", "ffn_up_dx")
    dx1, dout, G["norm_ffn_pre"], G["norm_mix_post"] = _mid_bwd_call(x1, P["norm_ffn_pre"] + zero, dh2, dx2, out, P["norm_mix_post"])
    zero = on_grad("w_out", shard_major(_matmul(mixed, dout, "tn", "mix_out_dw")))
    dmixed = _matmul(dout, P["w_out"], "nt", "mix_out_dx")
    dya, dyc, dga, dgc, dba, dbc = _mix_bwd_call(dmixed, proj, gate_col0, P["b_gate"] + zero, y_a, y_c)
    G["b_gate"] = jnp.concatenate([dba, dbc], axis=1)
    zero = on_grad("w_attn_out", _matmul(a_bf, dya, "tn", "attn_out_dw", out_shards=True))
    zero = zero + on_grad("conv_pw_w", shard_major(_matmul(cact, dyc, "tn", "conv_pw_dw")))
    da = _matmul(dya, P["w_attn_out"], "nt", "attn_out_dx")
    dcact = _matmul(dyc, P["conv_pw_w"], "nt", "conv_pw_dx")
    dc1, G["conv_ln_g"], G["conv_ln_b"] = _ln_silu_bwd_call(c1, P["conv_ln_g"] + zero, P["conv_ln_b"], dcact)
    dval, dgate, dw_dw, G["conv_dw_b"] = _conv_bwd_call(proj, glu_col0, P["conv_dw_w"], dc1)
    G["conv_dw_w"] = dw_dw[:CONV_WIDTH]
    delta = _attn_delta_call(a, da)
    dqs, dks, dvs, dbs = [], [], [], []
    for g in range(N_GROUPS):
        grouped = [_to_group_order(t, 0, g, f"group_order_{n}{g}")[0] for t, n in ((da, "da"), (lse, "lse"), (delta, "delta"))]
        dq, dk, dv, db = _attn_bwd_call(*qkv[g], bias, *grouped, g)
        dqs.append(_to_token_order(dq, g, f"token_order_dq{g}"))
        dks.append(_to_token_order(dk, g, f"token_order_dk{g}"))
        dvs.append(_to_token_order(dv, g, f"token_order_dv{g}"))
        dbs.append(db)
    G["rel_bias"] = _bias_grad_call(jnp.concatenate(dbs, axis=0), buckets)
    dproj = jnp.concatenate([t.astype(BF16) for t in dqs + dks + dvs] + [dval, dgate, dga, dgc], axis=1)
    zero = on_grad("w_in", _matmul(h1, dproj, "tn", "proj_in_dw", out_shards=True, tm=512))
    dh1 = _matmul(dproj, P["w_in"], "nt", "proj_in_dx")
    grad_x, G["norm_mix_pre"] = _in_bwd_call(xs, P["norm_mix_pre"] + zero, dh1, dx1)
    return loss_tile, grad_x, G


def kernel(x, w_in, b_gate, rel_bias, w_attn_out, conv_dw_w, conv_dw_b, conv_ln_g, conv_ln_b, conv_pw_w, w_out, norm_mix_pre, norm_mix_post, norm_ffn_pre, norm_ffn_post, w_up, ffn_conv_w, ffn_conv_b, w_down, loss_target, m_w_in, m_b_gate, m_rel_bias, m_w_attn_out, m_conv_dw_w, m_conv_dw_b, m_conv_ln_g, m_conv_ln_b, m_conv_pw_w, m_w_out, m_norm_mix_pre, m_norm_mix_post, m_norm_ffn_pre, m_norm_ffn_post, m_w_up, m_ffn_conv_w, m_ffn_conv_b, m_w_down, v_w_in, v_b_gate, v_rel_bias, v_w_attn_out, v_conv_dw_w, v_conv_dw_b, v_conv_ln_g, v_conv_ln_b, v_conv_pw_w, v_w_out, v_norm_mix_pre, v_norm_mix_post, v_norm_ffn_pre, v_norm_ffn_post, v_w_up, v_ffn_conv_w, v_ffn_conv_b, v_w_down):
    weights = dict(w_in=w_in, b_gate=b_gate, rel_bias=rel_bias, w_attn_out=w_attn_out, conv_dw_w=conv_dw_w, conv_dw_b=conv_dw_b,
                   conv_ln_g=conv_ln_g, conv_ln_b=conv_ln_b, conv_pw_w=conv_pw_w, w_out=w_out, norm_mix_pre=norm_mix_pre,
                   norm_mix_post=norm_mix_post, norm_ffn_pre=norm_ffn_pre, norm_ffn_post=norm_ffn_post, w_up=w_up,
                   ffn_conv_w=ffn_conv_w, ffn_conv_b=ffn_conv_b, w_down=w_down)
    m_in = dict(w_in=m_w_in, b_gate=m_b_gate, rel_bias=m_rel_bias, w_attn_out=m_w_attn_out, conv_dw_w=m_conv_dw_w,
                conv_dw_b=m_conv_dw_b, conv_ln_g=m_conv_ln_g, conv_ln_b=m_conv_ln_b, conv_pw_w=m_conv_pw_w, w_out=m_w_out,
                norm_mix_pre=m_norm_mix_pre, norm_mix_post=m_norm_mix_post, norm_ffn_pre=m_norm_ffn_pre,
                norm_ffn_post=m_norm_ffn_post, w_up=m_w_up, ffn_conv_w=m_ffn_conv_w, ffn_conv_b=m_ffn_conv_b, w_down=m_w_down)
    v_in = dict(w_in=v_w_in, b_gate=v_b_gate, rel_bias=v_rel_bias, w_attn_out=v_w_attn_out, conv_dw_w=v_conv_dw_w,
                conv_dw_b=v_conv_dw_b, conv_ln_g=v_conv_ln_g, conv_ln_b=v_conv_ln_b, conv_pw_w=v_conv_pw_w, w_out=v_w_out,
                norm_mix_pre=v_norm_mix_pre, norm_mix_post=v_norm_mix_post, norm_ffn_pre=v_norm_ffn_pre,
                norm_ffn_post=v_norm_ffn_post, w_up=v_w_up, ffn_conv_w=v_ffn_conv_w, ffn_conv_b=v_ffn_conv_b, w_down=v_w_down)
    names = list(weights)
    xi, yi, ci = _position()
    chip = 2 * xi + yi
    core_arr = jnp.reshape(ci, (1,)).astype(jnp.int32)

    xs = x[0]
    target = loss_target[0]
    S, D = xs.shape

    big = ["w_in", "w_attn_out", "conv_pw_w", "w_out", "w_up", "w_down"]
    row_sharded = ("conv_pw_w", "w_out", "w_down")
    bf16_shard = {k: weights[k][0].astype(BF16) for k in big}
    natural = lambda k, g: g.reshape(-1, g.shape[2]) if k in row_sharded else g
    w_in_full, dw4, fc4 = _allgather_call([bf16_shard["w_in"]], [conv_dw_w[0], ffn_conv_w[0]])
    late_sets = dict(mix=["w_attn_out", "conv_pw_w", "w_out"], ffn=["w_up", "w_down"])
    started, after = {}, w_in_full
    for tag, keys in late_sets.items():
        srcs = [bf16_shard[k] for k in keys]
        lands = [lax.empty((N_CHIPS,) + s.shape, BF16) for s in srcs]
        started[tag] = _split_start(f"gather_{tag}_start", srcs, lands, 4 * len(keys), _gather_copies, after)
        after = started[tag]["tile"]
    launched = started["mix"]["token"] + started["ffn"]["token"]

    def late_weights(tag, after):
        landed = _split_wait(f"gather_{tag}_wait", started[tag], _gather_copies, after)[len(late_sets[tag]):]
        return {k: natural(k, g) for k, g in zip(late_sets[tag], landed)}

    chip_core = jnp.stack([chip, ci]).astype(jnp.int32)
    exchanging, pending = {}, {}

    def pair_up(after):
        token = jnp.float32(0.0)
        for k in list(exchanging):
            g3, r1 = _split_wait(f"sibling_exchange_wait_{k}", exchanging.pop(k), _sibling_copies, after)
            s32, s16 = _pair_sum_call(g3, r1, core_arr, f"pair_sum_{k}")
            land = lax.empty((3,) + s16.shape[1:], BF16)
            pending[k] = (s32, _split_start(f"chip_exchange_start_{k}", [s16], [land], 3, _exchange_copies, s32))
            token = token + pending[k][1]["token"]
        return token

    def on_grad(k, g3):
        token = pair_up(g3[0, :SUBLANES, :LANES])
        land = lax.empty((N_CHIPS, g3.shape[1] // 2, g3.shape[2]), F32)
        exchanging[k] = _split_start(f"sibling_exchange_start_{k}", [g3], [land], 1, _sibling_copies, core_arr)
        return token + exchanging[k]["token"]

    def finish(keys, after, tag):
        halves = []
        for k in keys:
            s32, st = pending[k]
            recv2 = _split_wait(f"chip_exchange_wait_{k}", st, _exchange_copies, after)[1]
            halves.append(_chip_sum_call(s32, recv2, chip_core, f"chip_sum_{k}"))
        return dict(zip(keys, _sibling_assemble_call(halves, f"grad_sibling_assemble_{tag}")))

    P = dict(w_in=w_in_full, conv_dw_w=jnp.concatenate(list(dw4), axis=1), ffn_conv_w=jnp.concatenate(list(fc4), axis=1),
             b_gate=b_gate, rel_bias=rel_bias, conv_dw_b=conv_dw_b, conv_ln_g=conv_ln_g, conv_ln_b=conv_ln_b,
             norm_mix_pre=norm_mix_pre + launched, norm_mix_post=norm_mix_post, norm_ffn_pre=norm_ffn_pre,
             norm_ffn_post=norm_ffn_post, ffn_conv_b=ffn_conv_b)
    loss_tile, grad_x, G = _local_step(xs, target, P, late_weights, on_grad)

    small = [k for k in names if k not in big]
    packed = _pack([loss_tile[:1]] + [G[k] for k in small])
    summed_block = _small_allreduce_call(packed)
    summed_block = summed_block + pair_up(summed_block[:SUBLANES])
    loss_row, *summed = _unpack(summed_block, [(1, LANES)] + [G[k].shape for k in small])
    loss = loss_row[0, 0]
    reduced = {}
    for k, gsum in zip(small, summed):
        if k in ("conv_dw_w", "ffn_conv_w"):
            cols = weights[k].shape[2]
            reduced[k] = lax.dynamic_slice_in_dim(gsum, chip * cols, cols, axis=1)
        else:
            reduced[k] = gsum

    grads, deltas, new_m, new_v = {}, {}, {}, {}

    def update(keys):
        for k in keys:
            d, mn, vn = _adamw_call(weights[k][0], reduced[k], m_in[k][0], v_in[k][0], f"adamw_{k}")
            grads[k], deltas[k], new_m[k], new_v[k] = reduced[k][None], d[None], mn[None], vn[None]

    others = [k for k in big if k != "w_in"]
    reduced.update(finish(others, summed_block, "others"))
    update(others)
    reduced.update(finish(["w_in"], deltas["w_up"], "w_in"))
    update(["w_in"])
    flat2 = lambda t: t.reshape(-1, t.shape[-1]) if t.ndim == 3 else t
    pw = _pack([flat2(weights[k]) for k in small])
    pg = _pack([reduced[k] for k in small])
    pm = _pack([flat2(m_in[k]) for k in small])
    pv = _pack([flat2(v_in[k]) for k in small])
    pd, pmn, pvn = _adamw_call(pw, pg, pm, pv, "adamw_small")
    shapes = [weights[k].shape for k in small]
    for k, gk, dk_, mk, vk in zip(small, [reduced[k] for k in small], _unpack(pd, shapes), _unpack(pmn, shapes), _unpack(pvn, shapes)):
        grads[k], deltas[k], new_m[k], new_v[k] = gk.reshape(weights[k].shape), dk_, mk, vk

    return (loss, grad_x[None], *[grads[k] for k in names], *[deltas[k] for k in names],
            *[new_m[k] for k in names], *[new_v[k] for k in names])
```

```python
import functools
import math

import jax
import jax.numpy as jnp
import numpy as np
from jax import lax
from jax.experimental import pallas as pl
from jax.experimental.pallas import tpu as pltpu

F32 = jnp.float32
BF16 = jnp.bfloat16
MESH = pl.DeviceIdType.MESH

HEAD_DIM = 128
HEADS_PER_GROUP = 4
DILATED_PATTERNS = ((128, 1), (512, 4), (2048, 16))
N_GROUPS = 3
N_HEADS = N_GROUPS * HEADS_PER_GROUP
SPAN = 128
GROUP_WIDTH = HEADS_PER_GROUP * HEAD_DIM
CONV_WIDTH = 31
FFN_CONV_WIDTH = 3
N_BUCKETS = 32
MAX_DISTANCE = 2048
RMS_EPS = 1e-6
LN_EPS = 1e-5
NEG_INF = -1e30
ADAM_LR = 0.001
ADAM_B1 = 0.9
ADAM_B2 = 0.999
ADAM_EPS = 1e-08
ADAM_WD = 0.01
ADAM_STEP = 10

LANES = 128
SUBLANES = 8
ROW_TILE = 256
TIME_BLOCK = 128
CONV_PAD = 32
FFN_PAD = 8
VMEM_LIMIT = 56 << 20


def _params(sem=None, vmem=None):
    kw = {}
    if sem is not None:
        kw["dimension_semantics"] = sem
    if vmem is not None:
        kw["vmem_limit_bytes"] = vmem
    return pltpu.CompilerParams(**kw)


def _pick(n, cands):
    for c in cands:
        if n % c == 0:
            return c
    return n


ELEMENTWISE_TILE_BYTES = 1 << 20


def _row_tile(rows, cols):
    for c in (512, 256, 128, 64, 32, 16):
        if rows % c == 0 and c * cols * 4 <= ELEMENTWISE_TILE_BYTES:
            return c
    return 16 if rows % 16 == 0 else 8


N_CHIPS = 4
M_TILES = (1024, 1408, 512, 256, 128)
N_TILES = (512, 1408, 256, 128)
K_TILES = (2176, 2048, 1408, 1024, 512, 256, 128)


def _matmul(a, b, mode, name, out_shards=False, tm=None):
    assert a.dtype == BF16 and b.dtype == BF16, (name, a.dtype, b.dtype)
    b3 = b.ndim == 3
    tn = tk = None
    if mode == "nn":
        M, K = a.shape
        N = b.shape[-1] * (N_CHIPS if b3 else 1)
        tn = b.shape[-1] if b3 else None
    elif mode == "nt":
        M, K = a.shape
        N = b.shape[-2]
        tk = b.shape[-1] if b3 else None
    else:
        K, M = a.shape
        N = b.shape[1]
        tn = N // N_CHIPS if out_shards else None
    tm = tm or _pick(M, M_TILES)
    tn = tn or _pick(N, N_TILES)
    tk = tk or _pick(K, K_TILES)
    nk = K // tk
    dn = {"nn": (((1,), (0,)), ((), ())), "nt": (((1,), (1,)), ((), ())), "tn": (((0,), (0,)), ((), ()))}[mode]

    def body(a_ref, b_ref, o_ref):
        if nk == 1:
            o_ref[...] = lax.dot_general(a_ref[...], b_ref[...], dn, preferred_element_type=F32)
        else:
            @pl.when(pl.program_id(2) == 0)
            def _():
                o_ref[...] = jnp.zeros_like(o_ref)

            o_ref[...] += lax.dot_general(a_ref[...], b_ref[...], dn, preferred_element_type=F32)

    if mode == "tn":
        a_spec = pl.BlockSpec((tk, tm), lambda i, j, k: (k, i))
    else:
        a_spec = pl.BlockSpec((tm, tk), lambda i, j, k: (i, k))
    if mode == "nn":
        b_spec = pl.BlockSpec((None, tk, tn), lambda i, j, k: (j, k, 0)) if b3 else pl.BlockSpec((tk, tn), lambda i, j, k: (k, j))
    elif mode == "nt":
        b_spec = pl.BlockSpec((None, tn, tk), lambda i, j, k: (k, j, 0)) if b3 else pl.BlockSpec((tn, tk), lambda i, j, k: (j, k))
    else:
        b_spec = pl.BlockSpec((tk, tn), lambda i, j, k: (k, j))
    if out_shards:
        out_spec = pl.BlockSpec((None, tm, tn), lambda i, j, k: (j, i, 0))
        out_shape = jax.ShapeDtypeStruct((N_CHIPS, M, tn), F32)
    else:
        out_spec = pl.BlockSpec((tm, tn), lambda i, j, k: (i, j))
        out_shape = jax.ShapeDtypeStruct((M, N), F32)
    return pl.pallas_call(
        body, name=name, grid=(M // tm, N // tn, nk),
        in_specs=[a_spec, b_spec], out_specs=out_spec, out_shape=out_shape,
        compiler_params=_params(("parallel", "parallel", "arbitrary"), VMEM_LIMIT),
    )(a, b)


def _rms(x, g):
    r = lax.rsqrt(jnp.mean(x * x, axis=-1, keepdims=True) + RMS_EPS)
    return x * r * g


def _rms_bwd(x, g, dy):
    r = lax.rsqrt(jnp.mean(x * x, axis=-1, keepdims=True) + RMS_EPS)
    n = x * r
    dn = dy * g
    dx = r * (dn - n * jnp.mean(dn * n, axis=-1, keepdims=True))
    return dx, jnp.sum(dy * n, axis=0, keepdims=True)


def _sigmoid(x):
    return 1.0 / (1.0 + jnp.exp(-x))


_GELU_C = math.sqrt(2.0 / math.pi)


def _gelu(x):
    return 0.5 * x * (1.0 + jnp.tanh(_GELU_C * (x + 0.044715 * x * x * x)))


def _gelu_grad(x):
    t = jnp.tanh(_GELU_C * (x + 0.044715 * x * x * x))
    return 0.5 * (1.0 + t) + 0.5 * x * (1.0 - t * t) * _GELU_C * (1.0 + 3.0 * 0.044715 * x * x)


def _row_spec(width, col_block=0):
    return pl.BlockSpec((ROW_TILE, width), lambda i: (i, col_block))


def _vec_spec(width, col_block=0):
    return pl.BlockSpec((1, width), lambda i: (0, col_block))


def _accumulate(ref, part):
    @pl.when(pl.program_id(0) == 0)
    def _():
        ref[...] = part

    @pl.when(pl.program_id(0) > 0)
    def _():
        ref[...] += part


def _rms_fwd_call(x, g):
    S, D = x.shape

    def body(x_ref, g_ref, h_ref):
        h_ref[...] = _rms(x_ref[...], g_ref[...]).astype(BF16)

    return pl.pallas_call(
        body, name="rms_mix_pre", grid=(S // ROW_TILE,),
        in_specs=[_row_spec(D), _vec_spec(D)], out_specs=_row_spec(D),
        out_shape=jax.ShapeDtypeStruct((S, D), BF16),
        compiler_params=_params(("parallel",)),
    )(x, g)


def _ln_silu_call(c1, g, b):
    S, C = c1.shape

    def body(c_ref, g_ref, b_ref, o_ref):
        xv = c_ref[...]
        mu = jnp.mean(xv, axis=-1, keepdims=True)
        xc = xv - mu
        var = jnp.mean(xc * xc, axis=-1, keepdims=True)
        z = xc * lax.rsqrt(var + LN_EPS) * g_ref[...] + b_ref[...]
        o_ref[...] = (z * _sigmoid(z)).astype(BF16)

    return pl.pallas_call(
        body, name="conv_ln_silu", grid=(S // ROW_TILE,),
        in_specs=[_row_spec(C), _vec_spec(C), _vec_spec(C)], out_specs=_row_spec(C),
        out_shape=jax.ShapeDtypeStruct((S, C), BF16),
        compiler_params=_params(("parallel",)),
    )(c1, g, b)


def _ln_silu_bwd_call(c1, g, b, dc):
    S, C = c1.shape

    def body(c_ref, g_ref, b_ref, dc_ref, dx_ref, dg_ref, db_ref):
        xv = c_ref[...]
        mu = jnp.mean(xv, axis=-1, keepdims=True)
        xc = xv - mu
        rs = lax.rsqrt(jnp.mean(xc * xc, axis=-1, keepdims=True) + LN_EPS)
        xh = xc * rs
        z = xh * g_ref[...] + b_ref[...]
        sg = _sigmoid(z)
        dz = dc_ref[...] * (sg * (1.0 + z * (1.0 - sg)))
        dxh = dz * g_ref[...]
        dx_ref[...] = rs * (dxh - jnp.mean(dxh, axis=-1, keepdims=True) - xh * jnp.mean(dxh * xh, axis=-1, keepdims=True))
        _accumulate(dg_ref, jnp.sum(dz * xh, axis=0, keepdims=True))
        _accumulate(db_ref, jnp.sum(dz, axis=0, keepdims=True))

    return pl.pallas_call(
        body, name="conv_ln_silu_bwd", grid=(S // ROW_TILE,),
        in_specs=[_row_spec(C), _vec_spec(C), _vec_spec(C), _row_spec(C)],
        out_specs=[_row_spec(C), _vec_spec(C), _vec_spec(C)],
        out_shape=[jax.ShapeDtypeStruct((S, C), F32), jax.ShapeDtypeStruct((1, C), F32), jax.ShapeDtypeStruct((1, C), F32)],
        compiler_params=_params(("arbitrary",)),
    )(c1, g, b, dc)


def _mix_call(proj, gate_col0, b_gate, y_a, y_c):
    S, D = y_a.shape
    w = 512
    nc = D // w
    ga0, gc0 = gate_col0 // w, (gate_col0 + D) // w

    def body(ga_ref, gc_ref, ba_ref, bc_ref, ya_ref, yc_ref, o_ref):
        o_ref[...] = (_sigmoid(ga_ref[...] + ba_ref[...]) * ya_ref[...]
                      + _sigmoid(gc_ref[...] + bc_ref[...]) * yc_ref[...]).astype(BF16)

    tile = lambda off: pl.BlockSpec((ROW_TILE, w), lambda i, j: (i, off + j))
    vec = lambda off: pl.BlockSpec((1, w), lambda i, j: (0, off + j))
    return pl.pallas_call(
        body, name="gate_mix", grid=(S // ROW_TILE, nc),
        in_specs=[tile(ga0), tile(gc0), vec(0), vec(nc), tile(0), tile(0)],
        out_specs=tile(0), out_shape=jax.ShapeDtypeStruct((S, D), BF16),
        compiler_params=_params(("parallel", "parallel")),
    )(proj, proj, b_gate, b_gate, y_a, y_c)


def _mix_bwd_call(dmixed, proj, gate_col0, b_gate, y_a, y_c):
    S, D = y_a.shape
    w = 512
    nc = D // w
    ga0, gc0 = gate_col0 // w, (gate_col0 + D) // w

    def body(dm_ref, ga_ref, gc_ref, ba_ref, bc_ref, ya_ref, yc_ref, dya_ref, dyc_ref, dga_ref, dgc_ref, dba_ref, dbc_ref):
        dm = dm_ref[...]
        sa = _sigmoid(ga_ref[...] + ba_ref[...])
        sc = _sigmoid(gc_ref[...] + bc_ref[...])
        dya_ref[...] = (dm * sa).astype(BF16)
        dyc_ref[...] = (dm * sc).astype(BF16)
        dga = dm * ya_ref[...] * sa * (1.0 - sa)
        dgc = dm * yc_ref[...] * sc * (1.0 - sc)
        dga_ref[...] = dga.astype(BF16)
        dgc_ref[...] = dgc.astype(BF16)
        pa = jnp.sum(dga, axis=0, keepdims=True)
        pc = jnp.sum(dgc, axis=0, keepdims=True)

        @pl.when(pl.program_id(1) == 0)
        def _():
            dba_ref[...] = pa
            dbc_ref[...] = pc

        @pl.when(pl.program_id(1) > 0)
        def _():
            dba_ref[...] += pa
            dbc_ref[...] += pc

    tile = lambda off: pl.BlockSpec((ROW_TILE, w), lambda j, i: (i, off + j))
    vec = lambda off: pl.BlockSpec((1, w), lambda j, i: (0, off + j))
    return pl.pallas_call(
        body, name="gate_mix_bwd", grid=(nc, S // ROW_TILE),
        in_specs=[tile(0), tile(ga0), tile(gc0), vec(0), vec(nc), tile(0), tile(0)],
        out_specs=[tile(0), tile(0), tile(0), tile(0), vec(0), vec(0)],
        out_shape=[jax.ShapeDtypeStruct((S, D), BF16)] * 4 + [
                   jax.ShapeDtypeStruct((1, D), F32), jax.ShapeDtypeStruct((1, D), F32)],
        compiler_params=_params(("parallel", "arbitrary")),
    )(dmixed, proj, proj, b_gate, b_gate, y_a, y_c)


def _res1_call(x, out, g_post, g_pre):
    S, D = x.shape

    def body(x_ref, o_ref, gp_ref, gq_ref, x1_ref, h2_ref):
        x1 = x_ref[...] + _rms(o_ref[...], gp_ref[...])
        x1_ref[...] = x1
        h2_ref[...] = _rms(x1, gq_ref[...]).astype(BF16)

    return pl.pallas_call(
        body, name="residual_mix", grid=(S // ROW_TILE,),
        in_specs=[_row_spec(D), _row_spec(D), _vec_spec(D), _vec_spec(D)],
        out_specs=[_row_spec(D), _row_spec(D)],
        out_shape=[jax.ShapeDtypeStruct((S, D), F32), jax.ShapeDtypeStruct((S, D), BF16)],
        compiler_params=_params(("parallel",)),
    )(x, out, g_post, g_pre)


def _loss_call(y, x1, g_post, target):
    S, D = y.shape

    def body(y_ref, x1_ref, g_ref, t_ref, loss_ref, dx_ref):
        err = x1_ref[...] + _rms(y_ref[...], g_ref[...]) - t_ref[...]
        dx_ref[...] = err * (1.0 / D)
        part = 0.5 * jnp.sum(jnp.mean(err * err, axis=-1, keepdims=True), axis=0, keepdims=True)
        _accumulate(loss_ref, jnp.broadcast_to(part, (SUBLANES, LANES)))

    return pl.pallas_call(
        body, name="residual_ffn_loss", grid=(S // ROW_TILE,),
        in_specs=[_row_spec(D), _row_spec(D), _vec_spec(D), _row_spec(D)],
        out_specs=[pl.BlockSpec((SUBLANES, LANES), lambda i: (0, 0)), _row_spec(D)],
        out_shape=[jax.ShapeDtypeStruct((SUBLANES, LANES), F32), jax.ShapeDtypeStruct((S, D), F32)],
        compiler_params=_params(("arbitrary",)),
    )(y, x1, g_post, target)


def _rms_bwd_call(x, g, dy, name):
    S, D = x.shape

    def body(x_ref, g_ref, dy_ref, dx_ref, dg_ref):
        dx, dg = _rms_bwd(x_ref[...], g_ref[...], dy_ref[...])
        dx_ref[...] = dx.astype(BF16)
        _accumulate(dg_ref, dg)

    return pl.pallas_call(
        body, name=name, grid=(S // ROW_TILE,),
        in_specs=[_row_spec(D), _vec_spec(D), _row_spec(D)],
        out_specs=[_row_spec(D), _vec_spec(D)],
        out_shape=[jax.ShapeDtypeStruct((S, D), BF16), jax.ShapeDtypeStruct((1, D), F32)],
        compiler_params=_params(("arbitrary",)),
    )(x, g, dy)


def _mid_bwd_call(x1, g_pre, dh2, dx2, out, g_post):
    S, D = x1.shape

    def body(x1_ref, gq_ref, dh_ref, dx2_ref, o_ref, gp_ref, dx1_ref, do_ref, dgq_ref, dgp_ref):
        d, dgq = _rms_bwd(x1_ref[...], gq_ref[...], dh_ref[...])
        dx1 = dx2_ref[...] + d
        dx1_ref[...] = dx1
        do, dgp = _rms_bwd(o_ref[...], gp_ref[...], dx1)
        do_ref[...] = do.astype(BF16)
        _accumulate(dgq_ref, dgq)
        _accumulate(dgp_ref, dgp)

    return pl.pallas_call(
        body, name="residual_mix_bwd", grid=(S // ROW_TILE,),
        in_specs=[_row_spec(D), _vec_spec(D), _row_spec(D), _row_spec(D), _row_spec(D), _vec_spec(D)],
        out_specs=[_row_spec(D), _row_spec(D), _vec_spec(D), _vec_spec(D)],
        out_shape=[jax.ShapeDtypeStruct((S, D), F32), jax.ShapeDtypeStruct((S, D), BF16)] + [jax.ShapeDtypeStruct((1, D), F32)] * 2,
        compiler_params=_params(("arbitrary",)),
    )(x1, g_pre, dh2, dx2, out, g_post)


def _in_bwd_call(x, g, dh1, dx1):
    S, D = x.shape

    def body(x_ref, g_ref, dh_ref, dx1_ref, gx_ref, dg_ref):
        d, dg = _rms_bwd(x_ref[...], g_ref[...], dh_ref[...])
        gx_ref[...] = dx1_ref[...] + d
        _accumulate(dg_ref, dg)

    return pl.pallas_call(
        body, name="rms_mix_pre_bwd", grid=(S // ROW_TILE,),
        in_specs=[_row_spec(D), _vec_spec(D), _row_spec(D), _row_spec(D)],
        out_specs=[_row_spec(D), _vec_spec(D)],
        out_shape=[jax.ShapeDtypeStruct((S, D), F32), jax.ShapeDtypeStruct((1, D), F32)],
        compiler_params=_params(("arbitrary",)),
    )(x, g, dh1, dx1)


def _bucket_table(dilation):
    qi = np.arange(SPAN)[:, None]
    ki = np.arange(2 * SPAN)[None, :]
    dist = np.maximum(qi + SPAN - ki, 0) * dilation
    max_exact = N_BUCKETS // 2
    d = np.maximum(dist, 1).astype(np.float64)
    large = max_exact + (np.log(d / max_exact) / math.log(MAX_DISTANCE / max_exact) * (N_BUCKETS - max_exact)).astype(np.int32)
    large = np.minimum(large, N_BUCKETS - 1)
    return np.where(dist < max_exact, dist, large).astype(np.int32)


def _bucket_tables():
    return jnp.asarray(np.stack([_bucket_table(r) for _, r in DILATED_PATTERNS]))


def _bias_table_call(rel_bias, buckets):
    def body(rb_ref, bk_ref, o_ref):
        for h in range(N_HEADS):
            bk = bk_ref[h // HEADS_PER_GROUP]

            def step(b, acc):
                return jnp.where(bk == b, rb_ref[b, h], acc)

            o_ref[h] = lax.fori_loop(0, N_BUCKETS, step, jnp.zeros((SPAN, 2 * SPAN), F32))

    return pl.pallas_call(
        body, name="rel_bias_table",
        in_specs=[pl.BlockSpec(memory_space=pltpu.SMEM), pl.BlockSpec(memory_space=pltpu.VMEM)],
        out_specs=pl.BlockSpec(memory_space=pltpu.VMEM),
        out_shape=jax.ShapeDtypeStruct((N_HEADS, SPAN, 2 * SPAN), F32),
    )(rel_bias, buckets)


def _bias_grad_call(dbias, buckets):
    def body(db_ref, bk_ref, o_ref, rows_ref):
        for h in range(N_HEADS):
            bk = bk_ref[h // HEADS_PER_GROUP]
            dv = db_ref[h]

            def step(b, carry):
                rows_ref[h, b] = jnp.sum(jnp.where(bk == b, dv, 0.0), axis=0, keepdims=True)
                return carry

            lax.fori_loop(0, N_BUCKETS, step, 0)
        o_ref[...] = jnp.sum(rows_ref[...], axis=-1, keepdims=True)

    out = pl.pallas_call(
        body, name="rel_bias_grad",
        in_specs=[pl.BlockSpec(memory_space=pltpu.VMEM), pl.BlockSpec(memory_space=pltpu.VMEM)],
        out_specs=pl.BlockSpec(memory_space=pltpu.VMEM),
        out_shape=jax.ShapeDtypeStruct((N_HEADS, N_BUCKETS, 1, 1), F32),
        scratch_shapes=[pltpu.VMEM((N_HEADS, N_BUCKETS, 1, 2 * SPAN), F32)],
    )(dbias, buckets)
    return out.reshape(N_HEADS, N_BUCKETS).T


def _dot_nt(a, b):
    return lax.dot_general(a, b, (((1,), (1,)), ((), ())), preferred_element_type=F32)


def _dot_nn(a, b):
    return lax.dot_general(a, b, (((1,), (0,)), ((), ())), preferred_element_type=F32)


def _dot_tn(a, b):
    return lax.dot_general(a, b, (((0,), (0,)), ((), ())), preferred_element_type=F32)


def _band_masks(n, nb):
    qi = lax.broadcasted_iota(jnp.int32, (SPAN, SPAN), 0)
    ki = lax.broadcasted_iota(jnp.int32, (SPAN, SPAN), 1)
    prev_ok = jnp.logical_and(ki >= qi, n > 0)
    cur_ok = ki <= qi
    next_ok = jnp.logical_and(ki >= qi, n < nb - 1)
    return prev_ok, cur_ok, next_ok


def _attn_plan(S, group):
    r = DILATED_PATTERNS[group][1]
    hp, per = (HEADS_PER_GROUP, 1) if r == 1 else (2, 2)
    return r, S // (r * SPAN), hp, per


def _residue_rows(rho, r):
    return slice(None) if r == 1 else pl.ds(rho, SPAN, stride=r)


def _for_residues(r, per, fn):
    if r == per:
        for u in range(per):
            fn(u)
        return

    def step(i, carry):
        for u in range(per):
            fn(i * per + u)
        return carry

    lax.fori_loop(0, r // per, step, 0)


def _attn_fwd_call(proj, bias, group):
    S = proj.shape[0]
    r, nb, hp, per = _attn_plan(S, group)
    scale = HEAD_DIM ** -0.5
    kinds = ("q", "kp", "kc", "vp", "vc") if nb > 1 else ("q", "kc", "vc")

    def body(*refs):
        ins = {kind: refs[i * hp:(i + 1) * hp] for i, kind in enumerate(kinds)}
        b_ref, o_ref, lse_ref = refs[len(kinds) * hp:]
        n = pl.program_id(1)
        prev_ok, cur_ok, _ = _band_masks(n, nb)

        def residue(rho):
            rows = _residue_rows(rho, r)
            for j in range(hp):
                q = ins["q"][j][rows, :].astype(BF16)
                sc = jnp.where(cur_ok, _dot_nt(q, ins["kc"][j][rows, :].astype(BF16)) * scale + b_ref[j, :, SPAN:], NEG_INF)
                m = jnp.max(sc, axis=-1, keepdims=True)
                if nb > 1:
                    sp = jnp.where(prev_ok, _dot_nt(q, ins["kp"][j][rows, :].astype(BF16)) * scale + b_ref[j, :, :SPAN], NEG_INF)
                    m = jnp.maximum(m, jnp.max(sp, axis=-1, keepdims=True))
                pc = jnp.exp(sc - m)
                den = jnp.sum(pc, axis=-1, keepdims=True)
                acc = _dot_nn(pc.astype(BF16), ins["vc"][j][rows, :].astype(BF16))
                if nb > 1:
                    pp = jnp.exp(sp - m)
                    den = den + jnp.sum(pp, axis=-1, keepdims=True)
                    acc = acc + _dot_nn(pp.astype(BF16), ins["vp"][j][rows, :].astype(BF16))
                o_ref[j, rows, :] = acc / den
                lse_ref[j, rows, :] = jnp.broadcast_to(m + jnp.log(den), (SPAN, HEAD_DIM))

        _for_residues(r, per, residue)

    in_specs = [_head_spec(r, nb, hp, kind, group, jj) for kind in kinds for jj in range(hp)]
    in_specs.append(pl.BlockSpec((hp, SPAN, 2 * SPAN), lambda j, n: (group * (HEADS_PER_GROUP // hp) + j, 0, 0)))
    out = pl.BlockSpec((hp, r * SPAN, HEAD_DIM), lambda j, n: (j, n, 0))
    return pl.pallas_call(
        body, name=f"attn_fwd_g{group}", grid=(HEADS_PER_GROUP // hp, nb),
        in_specs=in_specs, out_specs=[out] * 2,
        out_shape=[jax.ShapeDtypeStruct((HEADS_PER_GROUP, S, HEAD_DIM), F32)] * 2,
        compiler_params=_params(("parallel", "parallel"), VMEM_LIMIT),
    )(*([proj] * (len(in_specs) - 1)), bias)


_PROJ_PART = dict(q=0, qn=0, kp=1, kc=1, vp=2, vc=2)


def _head_spec(r, nb, hp, kind, group, jj):
    if kind in _PROJ_PART:
        base = (_PROJ_PART[kind] * N_GROUPS + group) * HEADS_PER_GROUP
    else:
        base = 0
    if kind.endswith("p"):
        row = lambda n: jnp.maximum(n - 1, 0)
    elif kind.endswith("n"):
        row = lambda n: jnp.minimum(n + 1, nb - 1)
    else:
        row = lambda n: n
    return pl.BlockSpec((r * SPAN, HEAD_DIM), lambda j, n: (row(n), base + j * hp + jj))


def _attn_merge_call(parts):
    S = parts[0].shape[1]

    def body(o1, s1, o2, s2, o3, s3, a_ref, ab_ref, lse_ref):
        for j in range(HEADS_PER_GROUP):
            sl = slice(j * HEAD_DIM, (j + 1) * HEAD_DIM)
            mx = jnp.maximum(jnp.maximum(s1[j], s2[j]), s3[j])
            w1 = jnp.exp(s1[j] - mx)
            w2 = jnp.exp(s2[j] - mx)
            w3 = jnp.exp(s3[j] - mx)
            den = w1 + w2 + w3
            a = (w1 * o1[j] + w2 * o2[j] + w3 * o3[j]) / den
            a_ref[:, sl] = a
            ab_ref[:, sl] = a.astype(BF16)
            lse_ref[:, sl] = mx + jnp.log(den)

    heads = pl.BlockSpec((HEADS_PER_GROUP, ROW_TILE, HEAD_DIM), lambda i: (0, i, 0))
    return pl.pallas_call(
        body, name="attn_merge", grid=(S // ROW_TILE,),
        in_specs=[heads] * 6, out_specs=[_row_spec(GROUP_WIDTH)] * 3,
        out_shape=[jax.ShapeDtypeStruct((S, GROUP_WIDTH), F32), jax.ShapeDtypeStruct((S, GROUP_WIDTH), BF16),
                   jax.ShapeDtypeStruct((S, GROUP_WIDTH), F32)],
        compiler_params=_params(("parallel",)),
    )(*parts)


def _attn_delta_call(a, da):
    S = a.shape[0]

    def body(a_ref, da_ref, d_ref):
        for j in range(HEADS_PER_GROUP):
            sl = slice(j * HEAD_DIM, (j + 1) * HEAD_DIM)
            d = jnp.sum(a_ref[:, sl] * da_ref[:, sl], axis=-1, keepdims=True)
            d_ref[:, sl] = jnp.broadcast_to(d, (ROW_TILE, HEAD_DIM))

    return pl.pallas_call(
        body, name="attn_delta", grid=(S // ROW_TILE,),
        in_specs=[_row_spec(GROUP_WIDTH)] * 2, out_specs=_row_spec(GROUP_WIDTH),
        out_shape=jax.ShapeDtypeStruct((S, GROUP_WIDTH), F32),
        compiler_params=_params(("parallel",)),
    )(a, da)


def _attn_bwd_call(proj, bias, da, lse, delta, group):
    S = proj.shape[0]
    r, nb, hp, per = _attn_plan(S, group)
    scale = HEAD_DIM ** -0.5
    kinds = ("q", "qn", "kp", "kc", "vp", "vc", "da", "dan", "lse", "lsen", "dl", "dln") if nb > 1 else ("q", "kc", "vc", "da", "lse", "dl")
    source = dict(da=da, dan=da, lse=lse, lsen=lse, dl=delta, dln=delta)

    def body(*refs):
        ins = {kind: refs[i * hp:(i + 1) * hp] for i, kind in enumerate(kinds)}
        b_ref, dq_ref, dk_ref, dv_ref, db_ref = refs[len(kinds) * hp:]
        n = pl.program_id(1)
        prev_ok, cur_ok, next_ok = _band_masks(n, nb)

        @pl.when(n == 0)
        def _():
            db_ref[...] = jnp.zeros_like(db_ref)

        def residue(rho):
            rows = _residue_rows(rho, r)
            for j in range(hp):
                get = lambda kind: ins[kind][j][rows, :]
                q = get("q").astype(BF16)
                kc = get("kc").astype(BF16)
                vc = get("vc").astype(BF16)
                dav = get("da").astype(BF16)
                lse_q, dl_q = get("lse"), get("dl")
                pc = jnp.exp(jnp.where(cur_ok, _dot_nt(q, kc) * scale + b_ref[j, :, SPAN:], NEG_INF) - lse_q)
                dsc = pc * (_dot_nt(dav, vc) - dl_q)
                dsc_b = dsc.astype(BF16)
                dq = _dot_nn(dsc_b, kc)
                dk = _dot_tn(dsc_b, q)
                dv = _dot_tn(pc.astype(BF16), dav)
                db_ref[j, :, SPAN:] += dsc
                if nb > 1:
                    kp = get("kp").astype(BF16)
                    vp = get("vp").astype(BF16)
                    qn = get("qn").astype(BF16)
                    dan = get("dan").astype(BF16)
                    bp = b_ref[j, :, :SPAN]
                    pp = jnp.exp(jnp.where(prev_ok, _dot_nt(q, kp) * scale + bp, NEG_INF) - lse_q)
                    dsp = pp * (_dot_nt(dav, vp) - dl_q)
                    dq = dq + _dot_nn(dsp.astype(BF16), kp)
                    db_ref[j, :, :SPAN] += dsp
                    pn = jnp.exp(jnp.where(next_ok, _dot_nt(qn, kc) * scale + bp, NEG_INF) - get("lsen"))
                    dsn = pn * (_dot_nt(dan, vc) - get("dln"))
                    dk = dk + _dot_tn(dsn.astype(BF16), qn)
                    dv = dv + _dot_tn(pn.astype(BF16), dan)
                dq_ref[j, rows, :] = dq * scale
                dk_ref[j, rows, :] = dk * scale
                dv_ref[j, rows, :] = dv

        _for_residues(r, per, residue)

    per_group = HEADS_PER_GROUP // hp
    band = (hp, SPAN, 2 * SPAN)
    in_specs = [_head_spec(r, nb, hp, kind, group, jj) for kind in kinds for jj in range(hp)]
    in_specs.append(pl.BlockSpec(band, lambda j, n: (group * per_group + j, 0, 0)))
    operands = [source.get(kind, proj) for kind in kinds for _ in range(hp)] + [bias]
    out = pl.BlockSpec((hp, r * SPAN, HEAD_DIM), lambda j, n: (j, n, 0))
    return pl.pallas_call(
        body, name=f"attn_bwd_g{group}", grid=(per_group, nb),
        in_specs=in_specs,
        out_specs=[out] * 3 + [pl.BlockSpec(band, lambda j, n: (j, 0, 0))],
        out_shape=[jax.ShapeDtypeStruct((HEADS_PER_GROUP, S, HEAD_DIM), F32)] * 3
        + [jax.ShapeDtypeStruct((HEADS_PER_GROUP, SPAN, 2 * SPAN), F32)],
        compiler_params=_params(("parallel", "arbitrary"), VMEM_LIMIT),
    )(*operands)


def _dproj_call(dqkv, tails):
    S = tails[0].shape[0]
    width = len(dqkv) * GROUP_WIDTH + sum(t.shape[1] for t in tails)

    def body(*refs):
        o_ref = refs[-1]
        col = 0
        for ref in refs[:len(dqkv)]:
            for j in range(HEADS_PER_GROUP):
                o_ref[:, col:col + HEAD_DIM] = ref[j].astype(BF16)
                col += HEAD_DIM
        for ref in refs[len(dqkv):-1]:
            o_ref[:, col:col + ref.shape[1]] = ref[...]
            col += ref.shape[1]

    heads = pl.BlockSpec((HEADS_PER_GROUP, ROW_TILE, HEAD_DIM), lambda i: (0, i, 0))
    return pl.pallas_call(
        body, name="dproj_assemble", grid=(S // ROW_TILE,),
        in_specs=[heads] * len(dqkv) + [_row_spec(t.shape[1]) for t in tails],
        out_specs=_row_spec(width), out_shape=jax.ShapeDtypeStruct((S, width), BF16),
        compiler_params=_params(("parallel",)),
    )(*dqkv, *tails)


def _taps(width):
    return [(k, (width - 1 - k) // SUBLANES, (width - 1 - k) % SUBLANES) for k in range(width)]


def _shifted(win, width, pad, up):
    total = win.shape[0]
    for b in range(SUBLANES):
        taps = [(k, a) for k, a, bb in _taps(width) if bb == b]
        if not taps:
            continue
        if up:
            rolled = win if b == 0 else pltpu.roll(win, total - b, axis=0)
        else:
            rolled = win if b == 0 else pltpu.roll(win, b, axis=0)
        for k, a in taps:
            start = SUBLANES * a if up else pad - SUBLANES * a
            yield k, rolled[start:start + TIME_BLOCK, :]


def _conv_block(win, w_ref, width, pad):
    acc = None
    for k, rows in _shifted(win, width, pad, up=False):
        term = w_ref[k:k + 1, :] * rows
        acc = term if acc is None else acc + term
    return acc


def _conv_transpose_block(win, w_ref, width, pad):
    acc = None
    for k, rows in _shifted(win, width, pad, up=True):
        term = w_ref[k:k + 1, :] * rows
        acc = term if acc is None else acc + term
    return acc


def _conv_weight_grad(win, dy, dw_ref, width, pad):
    for k, rows in _shifted(win, width, pad, up=False):
        dw_ref[k:k + 1, :] += jnp.sum(dy * rows, axis=0, keepdims=True)


def _time_loop(S, step):
    def it(tb, carry):
        step(pl.multiple_of(tb * TIME_BLOCK, TIME_BLOCK))
        return carry

    lax.fori_loop(0, S // TIME_BLOCK, it, 0)


def _conv_fwd_call(proj, col0, w, b):
    S = proj.shape[0]
    C = w.shape[1]
    nt = C // LANES
    v0, g0 = col0 // LANES, (col0 + C) // LANES

    def body(val_ref, gate_ref, w_ref, b_ref, o_ref, pad_ref):
        pad_ref[0:CONV_PAD, :] = jnp.zeros((CONV_PAD, LANES), F32)
        pad_ref[CONV_PAD:, :] = val_ref[...] * _sigmoid(gate_ref[...])

        def step(t0):
            win = pad_ref[pl.ds(t0, TIME_BLOCK + CONV_PAD), :]
            o_ref[pl.ds(t0, TIME_BLOCK), :] = _conv_block(win, w_ref, CONV_WIDTH, CONV_PAD) + b_ref[...]

        _time_loop(S, step)

    seq = lambda off: pl.BlockSpec((S, LANES), lambda i: (0, off + i))
    return pl.pallas_call(
        body, name="conv_module", grid=(nt,),
        in_specs=[seq(v0), seq(g0), pl.BlockSpec((CONV_WIDTH, LANES), lambda i: (0, i)), pl.BlockSpec((1, LANES), lambda i: (0, i))],
        out_specs=seq(0), out_shape=jax.ShapeDtypeStruct((S, C), F32),
        scratch_shapes=[pltpu.VMEM((S + CONV_PAD, LANES), F32)],
        compiler_params=_params(("parallel",)),
    )(proj, proj, w, b)


def _conv_bwd_call(proj, col0, w, dc1):
    S = proj.shape[0]
    C = w.shape[1]
    nt = C // LANES
    v0, g0 = col0 // LANES, (col0 + C) // LANES

    def body(val_ref, gate_ref, w_ref, dy_ref, dval_ref, dgate_ref, dw_ref, db_ref, xpad_ref, dpad_ref, dwacc_ref):
        xpad_ref[0:CONV_PAD, :] = jnp.zeros((CONV_PAD, LANES), F32)
        xpad_ref[CONV_PAD:, :] = val_ref[...] * _sigmoid(gate_ref[...])
        dpad_ref[0:S, :] = dy_ref[...]
        dpad_ref[S:, :] = jnp.zeros((CONV_PAD, LANES), F32)
        dwacc_ref[...] = jnp.zeros_like(dwacc_ref)

        def step(t0):
            rows = pl.ds(t0, TIME_BLOCK)
            _conv_weight_grad(xpad_ref[pl.ds(t0, TIME_BLOCK + CONV_PAD), :], dy_ref[rows, :], dwacc_ref, CONV_WIDTH, CONV_PAD)
            dc0 = _conv_transpose_block(dpad_ref[pl.ds(t0, TIME_BLOCK + CONV_PAD), :], w_ref, CONV_WIDTH, CONV_PAD)
            sg = _sigmoid(gate_ref[rows, :])
            dval_ref[rows, :] = (dc0 * sg).astype(BF16)
            dgate_ref[rows, :] = (dc0 * val_ref[rows, :] * sg * (1.0 - sg)).astype(BF16)

        _time_loop(S, step)
        dw_ref[...] = dwacc_ref[...]
        db_ref[...] = jnp.sum(dy_ref[...], axis=0, keepdims=True)

    seq = lambda off: pl.BlockSpec((S, LANES), lambda i: (0, off + i))
    return pl.pallas_call(
        body, name="conv_module_bwd", grid=(nt,),
        in_specs=[seq(v0), seq(g0), pl.BlockSpec((CONV_WIDTH, LANES), lambda i: (0, i)), seq(0)],
        out_specs=[seq(0), seq(0), pl.BlockSpec((CONV_PAD, LANES), lambda i: (0, i)), pl.BlockSpec((1, LANES), lambda i: (0, i))],
        out_shape=[jax.ShapeDtypeStruct((S, C), BF16), jax.ShapeDtypeStruct((S, C), BF16),
                   jax.ShapeDtypeStruct((CONV_PAD, C), F32), jax.ShapeDtypeStruct((1, C), F32)],
        scratch_shapes=[pltpu.VMEM((S + CONV_PAD, LANES), F32), pltpu.VMEM((S + CONV_PAD, LANES), F32),
                        pltpu.VMEM((CONV_PAD, LANES), F32)],
        compiler_params=_params(("parallel",)),
    )(proj, proj, w, dc1)


def _ffn_fwd_call(u, w, b):
    S, C2 = u.shape
    C = C2 // 2
    nt = C // LANES

    def body(ug_ref, uv_ref, wg_ref, wv_ref, bg_ref, bv_ref, f_ref, pg_ref, pv_ref):
        zeros = jnp.zeros((FFN_PAD, LANES), F32)
        pg_ref[0:FFN_PAD, :] = zeros
        pv_ref[0:FFN_PAD, :] = zeros
        pg_ref[FFN_PAD:, :] = ug_ref[...]
        pv_ref[FFN_PAD:, :] = uv_ref[...]

        def step(t0):
            win = pl.ds(t0, TIME_BLOCK + FFN_PAD)
            cg = _conv_block(pg_ref[win, :], wg_ref, FFN_CONV_WIDTH, FFN_PAD) + bg_ref[...]
            cv = _conv_block(pv_ref[win, :], wv_ref, FFN_CONV_WIDTH, FFN_PAD) + bv_ref[...]
            f_ref[pl.ds(t0, TIME_BLOCK), :] = (_gelu(cg) * cv).astype(BF16)

        _time_loop(S, step)

    seq = lambda off: pl.BlockSpec((S, LANES), lambda i: (0, off + i))
    wsp = lambda off: pl.BlockSpec((FFN_CONV_WIDTH, LANES), lambda i: (0, off + i))
    bsp = lambda off: pl.BlockSpec((1, LANES), lambda i: (0, off + i))
    return pl.pallas_call(
        body, name="ffn_conv_geglu", grid=(nt,),
        in_specs=[seq(0), seq(nt), wsp(0), wsp(nt), bsp(0), bsp(nt)],
        out_specs=seq(0), out_shape=jax.ShapeDtypeStruct((S, C), BF16),
        scratch_shapes=[pltpu.VMEM((S + FFN_PAD, LANES), F32)] * 2,
        compiler_params=_params(("parallel",)),
    )(u, u, w, w, b, b)


def _ffn_bwd_call(u, w, b, df):
    S, C2 = u.shape
    C = C2 // 2
    nt = C // LANES

    def body(ug_ref, uv_ref, wg_ref, wv_ref, bg_ref, bv_ref, df_ref,
             dug_ref, duv_ref, dwg_ref, dwv_ref, dbg_ref, dbv_ref,
             pg_ref, pv_ref, dg_ref, dv_ref, dwg_acc, dwv_acc, dbg_acc, dbv_acc):
        zeros = jnp.zeros((FFN_PAD, LANES), F32)
        pg_ref[0:FFN_PAD, :] = zeros
        pv_ref[0:FFN_PAD, :] = zeros
        pg_ref[FFN_PAD:, :] = ug_ref[...]
        pv_ref[FFN_PAD:, :] = uv_ref[...]
        dg_ref[S:, :] = zeros
        dv_ref[S:, :] = zeros
        dwg_acc[...] = jnp.zeros_like(dwg_acc)
        dwv_acc[...] = jnp.zeros_like(dwv_acc)
        dbg_acc[...] = jnp.zeros_like(dbg_acc)
        dbv_acc[...] = jnp.zeros_like(dbv_acc)

        def first(t0):
            win = pl.ds(t0, TIME_BLOCK + FFN_PAD)
            rows = pl.ds(t0, TIME_BLOCK)
            xg = pg_ref[win, :]
            xv = pv_ref[win, :]
            cg = _conv_block(xg, wg_ref, FFN_CONV_WIDTH, FFN_PAD) + bg_ref[...]
            cv = _conv_block(xv, wv_ref, FFN_CONV_WIDTH, FFN_PAD) + bv_ref[...]
            dfb = df_ref[rows, :]
            dcg = dfb * cv * _gelu_grad(cg)
            dcv = dfb * _gelu(cg)
            dg_ref[rows, :] = dcg
            dv_ref[rows, :] = dcv
            _conv_weight_grad(xg, dcg, dwg_acc, FFN_CONV_WIDTH, FFN_PAD)
            _conv_weight_grad(xv, dcv, dwv_acc, FFN_CONV_WIDTH, FFN_PAD)
            dbg_acc[...] += jnp.sum(dcg, axis=0, keepdims=True)
            dbv_acc[...] += jnp.sum(dcv, axis=0, keepdims=True)

        def second(t0):
            win = pl.ds(t0, TIME_BLOCK + FFN_PAD)
            rows = pl.ds(t0, TIME_BLOCK)
            dug_ref[rows, :] = _conv_transpose_block(dg_ref[win, :], wg_ref, FFN_CONV_WIDTH, FFN_PAD).astype(BF16)
            duv_ref[rows, :] = _conv_transpose_block(dv_ref[win, :], wv_ref, FFN_CONV_WIDTH, FFN_PAD).astype(BF16)

        _time_loop(S, first)
        _time_loop(S, second)
        dwg_ref[...] = dwg_acc[...]
        dwv_ref[...] = dwv_acc[...]
        dbg_ref[...] = dbg_acc[...]
        dbv_ref[...] = dbv_acc[...]

    seq = lambda off: pl.BlockSpec((S, LANES), lambda i: (0, off + i))
    wsp = lambda off: pl.BlockSpec((FFN_CONV_WIDTH, LANES), lambda i: (0, off + i))
    bsp = lambda off: pl.BlockSpec((1, LANES), lambda i: (0, off + i))
    return pl.pallas_call(
        body, name="ffn_conv_geglu_bwd", grid=(nt,),
        in_specs=[seq(0), seq(nt), wsp(0), wsp(nt), bsp(0), bsp(nt), seq(0)],
        out_specs=[seq(0), seq(0), pl.BlockSpec((SUBLANES, LANES), lambda i: (0, i)), pl.BlockSpec((SUBLANES, LANES), lambda i: (0, i)),
                   bsp(0), bsp(0)],
        out_shape=[jax.ShapeDtypeStruct((S, C), BF16)] * 2 + [jax.ShapeDtypeStruct((SUBLANES, C), F32)] * 2
        + [jax.ShapeDtypeStruct((1, C), F32)] * 2,
        scratch_shapes=[pltpu.VMEM((S + FFN_PAD, LANES), F32)] * 4 + [pltpu.VMEM((SUBLANES, LANES), F32)] * 2
        + [pltpu.VMEM((1, LANES), F32)] * 2,
        compiler_params=_params(("parallel",)),
    )(u, u, w, w, b, b, df)


def _adamw_call(w, g, m, v, name):
    R, C = w.shape
    tr = _row_tile(R, C)
    c1 = 1.0 / (1.0 - ADAM_B1 ** ADAM_STEP)
    c2 = 1.0 / (1.0 - ADAM_B2 ** ADAM_STEP)

    def body(w_ref, g_ref, m_ref, v_ref, d_ref, mo_ref, vo_ref):
        gv = g_ref[...]
        mn = ADAM_B1 * m_ref[...] + (1.0 - ADAM_B1) * gv
        vn = ADAM_B2 * v_ref[...] + (1.0 - ADAM_B2) * (gv * gv)
        mo_ref[...] = mn
        vo_ref[...] = vn
        d_ref[...] = -ADAM_LR * ((mn * c1) / (jnp.sqrt(vn * c2) + ADAM_EPS) + ADAM_WD * w_ref[...])

    spec = pl.BlockSpec((tr, C), lambda i: (i, 0))
    return pl.pallas_call(
        body, name=name, grid=(R // tr,),
        in_specs=[spec] * 4, out_specs=[spec] * 3,
        out_shape=[jax.ShapeDtypeStruct((R, C), F32)] * 3,
        compiler_params=_params(("parallel",)),
    )(w, g, m, v)


def _position():
    return lax.axis_index("x"), lax.axis_index("y"), lax.axis_index("c")


def _chip_peers(x, y):
    return [(x, 1 - y), (1 - x, y), (1 - x, 1 - y)]


def _half_rows(ref, core, rows):
    h = rows // 2
    start = pl.multiple_of(core * h, 16)
    return ref.at[pl.ds(start, h), :] if len(ref.shape) == 2 else ref.at[:, pl.ds(start, h), :]


def _shard_half(ref, shard, core, rows):
    h = rows // 2
    return ref.at[shard, pl.ds(pl.multiple_of(core * h, 16), h), :]


ANY = pl.BlockSpec(memory_space=pl.ANY)


def _allgather_call(shards, whole):
    n, nw = len(shards), len(whole)
    outs_shape = [jax.ShapeDtypeStruct((N_CHIPS,) + s.shape, s.dtype) for s in shards + whole]

    def body(*refs):
        ins, outs = refs[:n + nw], refs[n + nw:2 * (n + nw)]
        send_sems, recv_sems, pass_send, pass_recv, own_send, own_recv = refs[2 * (n + nw):]
        x, y, c = _position()
        chip = 2 * x + y
        peers = _chip_peers(x, y)
        sent, local = [], []
        for i in range(n + nw):
            cp = pltpu.make_async_remote_copy(src_ref=ins[i], dst_ref=outs[i].at[chip], send_sem=own_send.at[i],
                                              recv_sem=own_recv.at[i], device_id=(x, y, 1 - c), device_id_type=MESH)
            cp.start()
            local.append(cp)
            rows = ins[i].shape[0]
            for k, (px, py) in enumerate(peers):
                if i < n:
                    src, dst = _half_rows(ins[i], c, rows), _shard_half(outs[i], chip, c, rows)
                else:
                    src, dst = ins[i], outs[i].at[chip]
                cp = pltpu.make_async_remote_copy(src_ref=src, dst_ref=dst, send_sem=send_sems.at[i, k],
                                                  recv_sem=recv_sems.at[i, k], device_id=(px, py, c), device_id_type=MESH)
                cp.start()
                sent.append(cp)
        passed = []
        for i in range(n + nw):
            rows = ins[i].shape[0]
            for k, (px, py) in enumerate(peers):
                landed = _shard_half(outs[i], 2 * px + py, c, rows) if i < n else outs[i].at[2 * px + py]
                pltpu.make_async_remote_copy(src_ref=landed, dst_ref=landed, send_sem=send_sems.at[i, k],
                                             recv_sem=recv_sems.at[i, k], device_id=(px, py, c), device_id_type=MESH).wait_recv()
                if i < n:
                    cp = pltpu.make_async_remote_copy(src_ref=landed, dst_ref=landed, send_sem=pass_send.at[i, k],
                                                      recv_sem=pass_recv.at[i, k], device_id=(x, y, 1 - c), device_id_type=MESH)
                    cp.start()
                    passed.append(cp)
        for cp in sent:
            cp.wait_send()
        for cp in passed:
            cp.wait()
        for cp in local:
            cp.wait()

    return pl.pallas_call(
        body, name="weight_allgather",
        in_specs=[ANY] * (n + nw), out_specs=[ANY] * (n + nw), out_shape=outs_shape,
        scratch_shapes=[pltpu.SemaphoreType.DMA((n + nw, 3)), pltpu.SemaphoreType.DMA((n + nw, 3)),
                        pltpu.SemaphoreType.DMA((n, 3)), pltpu.SemaphoreType.DMA((n, 3)),
                        pltpu.SemaphoreType.DMA((n + nw,)), pltpu.SemaphoreType.DMA((n + nw,))],
    )(*shards, *whole)


HBM_SPEC = pl.BlockSpec(memory_space=pltpu.HBM)
SEM_SPEC = pl.BlockSpec(memory_space=pltpu.SEMAPHORE)
DATAFLOW = pltpu.SideEffectType.DATAFLOW_SIDE_EFFECTING


def _in_hbm(a):
    return pltpu.with_memory_space_constraint(a, pltpu.HBM)


def _split_start(name, srcs, lands, n_sems, copies, after):
    n, m = len(srcs), len(lands)

    def body(*refs):
        src_refs, land_refs = refs[:n], refs[n:n + m]
        send_sem, recv_sem = refs[n + m + 1], refs[n + m + 2]
        token = refs[-1]
        for src, dst, dev, idx in copies(src_refs, land_refs):
            pltpu.make_async_remote_copy(src_ref=src, dst_ref=dst, send_sem=send_sem.at[idx], recv_sem=recv_sem.at[idx],
                                         device_id=dev, device_id_type=MESH).start()
        token[...] = jnp.zeros_like(token)

    outs = pl.pallas_call(
        body, name=name,
        in_specs=[HBM_SPEC] * (n + m) + [ANY],
        out_specs=[SEM_SPEC, SEM_SPEC] + [HBM_SPEC] * (n + m) + [pl.BlockSpec(memory_space=pltpu.VMEM)],
        out_shape=[pltpu.SemaphoreType.DMA((n_sems,)), pltpu.SemaphoreType.DMA((n_sems,))]
        + [pltpu.HBM(a.shape, a.dtype) for a in list(srcs) + list(lands)] + [jax.ShapeDtypeStruct((SUBLANES, LANES), F32)],
        input_output_aliases={i: 2 + i for i in range(n + m)},
        compiler_params=pltpu.CompilerParams(has_side_effects=DATAFLOW),
    )(*[_in_hbm(a) for a in list(srcs) + list(lands)], after)
    return dict(send=outs[0], recv=outs[1], srcs=list(outs[2:2 + n]), lands=list(outs[2 + n:2 + n + m]),
                tile=outs[-1], token=outs[-1][0, 0])


def _split_wait(name, started, copies, after):
    n, m = len(started["srcs"]), len(started["lands"])

    def body(*refs):
        src_refs, land_refs = refs[:n], refs[n:n + m]
        send_sem, recv_sem = refs[n + m], refs[n + m + 1]
        for src, dst, dev, idx in copies(src_refs, land_refs):
            cp = pltpu.make_async_remote_copy(src_ref=src, dst_ref=dst, send_sem=send_sem.at[idx], recv_sem=recv_sem.at[idx],
                                              device_id=dev, device_id_type=MESH)
            cp.wait_send()
            cp.wait_recv()

    arrays = started["srcs"] + started["lands"]
    outs = pl.pallas_call(
        body, name=name,
        in_specs=[HBM_SPEC] * (n + m) + [SEM_SPEC, SEM_SPEC, ANY],
        out_specs=[HBM_SPEC] * (n + m),
        out_shape=[pltpu.HBM(a.shape, a.dtype) for a in arrays],
        input_output_aliases={i: i for i in range(n + m)},
        compiler_params=pltpu.CompilerParams(has_side_effects=DATAFLOW),
    )(*arrays, started["send"], started["recv"], after)
    return list(outs)


def _gather_copies(srcs, lands):
    x, y, c = _position()
    chip = 2 * x + y
    targets = [(px, py, c) for px, py in _chip_peers(x, y)] + [(x, y, 1 - c)]
    return [(s, l.at[chip], dev, len(targets) * i + k) for i, (s, l) in enumerate(zip(srcs, lands)) for k, dev in enumerate(targets)]


def _sibling_copies(srcs, lands):
    x, y, c = _position()
    return [(_half_rows(srcs[0], 1 - c, srcs[0].shape[1]), lands[0], (x, y, 1 - c), 0)]


def _exchange_copies(srcs, lands):
    x, y, c = _position()
    return [(srcs[0].at[2 * px + py], lands[0].at[k], (px, py, c), k) for k, (px, py) in enumerate(_chip_peers(x, y))]


def _pair_sum_call(grad, recv, core, name):
    _, h, B = recv.shape
    tr = _row_tile(h, B)

    def body(core_ref, g_ref, r_ref, o_ref, ob_ref):
        s = g_ref[...] + r_ref[...]
        o_ref[...] = s
        ob_ref[...] = s.astype(BF16)

    g_spec = pl.BlockSpec((None, tr, B), lambda q, i, core_ref: (q, core_ref[0] * (h // tr) + i, 0))
    spec = pl.BlockSpec((None, tr, B), lambda q, i, core_ref: (q, i, 0))
    return pl.pallas_call(
        body, name=name,
        grid_spec=pltpu.PrefetchScalarGridSpec(num_scalar_prefetch=1, grid=(N_CHIPS, h // tr), in_specs=[g_spec, spec],
                                               out_specs=[spec, spec]),
        out_shape=[jax.ShapeDtypeStruct(recv.shape, F32), jax.ShapeDtypeStruct(recv.shape, BF16)],
        compiler_params=_params(("parallel", "parallel")),
    )(core, grad, recv)


def _chip_sum_call(partial, recv, chip_core, name):
    _, h, B = recv.shape
    tr = _row_tile(h, B)

    def body(cc_ref, p_ref, r_ref, o_ref):
        o_ref[...] = ((p_ref[...] + r_ref[0].astype(F32)) + r_ref[1].astype(F32)) + r_ref[2].astype(F32)

    return pl.pallas_call(
        body, name=name,
        grid_spec=pltpu.PrefetchScalarGridSpec(
            num_scalar_prefetch=1, grid=(h // tr,),
            in_specs=[pl.BlockSpec((None, tr, B), lambda i, cc_ref: (cc_ref[0], i, 0)),
                      pl.BlockSpec((3, tr, B), lambda i, cc_ref: (0, i, 0))],
            out_specs=pl.BlockSpec((tr, B), lambda i, cc_ref: (cc_ref[1] * (h // tr) + i, 0))),
        out_shape=jax.ShapeDtypeStruct((2 * h, B), F32),
        compiler_params=_params(("parallel",)),
    )(chip_core, partial, recv)


def _sibling_assemble_call(shards, name="grad_sibling_assemble"):
    n = len(shards)

    def body(*refs):
        ins, outs = refs[:n], refs[n:2 * n]
        send_sems, recv_sems = refs[2 * n:]
        x, y, c = _position()
        copies = []
        for i in range(n):
            rows = shards[i].shape[0]
            cp = pltpu.make_async_remote_copy(src_ref=_half_rows(ins[i], c, rows), dst_ref=_half_rows(outs[i], c, rows),
                                              send_sem=send_sems.at[i], recv_sem=recv_sems.at[i],
                                              device_id=(x, y, 1 - c), device_id_type=MESH)
            cp.start()
            copies.append(cp)
        for cp in copies:
            cp.wait()

    return pl.pallas_call(
        body, name=name,
        in_specs=[ANY] * n, out_specs=[ANY] * n,
        out_shape=[jax.ShapeDtypeStruct(s.shape, F32) for s in shards],
        input_output_aliases={i: i for i in range(n)},
        scratch_shapes=[pltpu.SemaphoreType.DMA((n,)), pltpu.SemaphoreType.DMA((n,))],
    )(*shards)


def _small_allreduce_call(packed):
    rows = packed.shape[0]

    def body(x_ref, o_ref, buf_ref, send_sems, recv_sems):
        x, y, c = _position()
        me = 4 * x + 2 * y + c
        buf_ref[me] = x_ref[...]
        copies = []
        for k in range(1, 8):
            peer = (1 - x if k & 4 else x, 1 - y if k & 2 else y, 1 - c if k & 1 else c)
            cp = pltpu.make_async_remote_copy(src_ref=buf_ref.at[me], dst_ref=buf_ref.at[me], send_sem=send_sems.at[k - 1],
                                              recv_sem=recv_sems.at[k - 1], device_id=peer, device_id_type=MESH)
            cp.start()
            copies.append(cp)
        for cp in copies:
            cp.wait()
        acc = buf_ref[0]
        for d in range(1, 8):
            acc = acc + buf_ref[d]
        o_ref[...] = acc

    return pl.pallas_call(
        body, name="small_grad_allreduce",
        in_specs=[pl.BlockSpec(memory_space=pltpu.VMEM)], out_specs=pl.BlockSpec(memory_space=pltpu.VMEM),
        out_shape=jax.ShapeDtypeStruct((rows, LANES), F32),
        scratch_shapes=[pltpu.VMEM((8, rows, LANES), F32), pltpu.SemaphoreType.DMA((7,)), pltpu.SemaphoreType.DMA((7,))],
    )(packed)


def _pack(arrays):
    flat = jnp.concatenate([a.reshape(-1).astype(F32) for a in arrays])
    rows = -(-flat.shape[0] // LANES)
    rows = -(-rows // SUBLANES) * SUBLANES
    flat = jnp.pad(flat, (0, rows * LANES - flat.shape[0]))
    return flat.reshape(rows, LANES)


def _unpack(packed, shapes):
    flat = packed.reshape(-1)
    out, off = [], 0
    for shp in shapes:
        size = int(np.prod(shp))
        out.append(flat[off:off + size].reshape(shp))
        off += size
    return out


def _local_step(xs, target, P, late_weights, on_grad):
    S, D = xs.shape
    qkv_width = 3 * N_HEADS * HEAD_DIM
    glu_col0, gate_col0 = qkv_width, qkv_width + 2 * D
    shard_major = lambda g: g.reshape(N_CHIPS, g.shape[0] // N_CHIPS, g.shape[1])

    h1 = _rms_fwd_call(xs, P["norm_mix_pre"])
    proj = _matmul(h1, P["w_in"], "nn", "proj_in")
    buckets = _bucket_tables()
    bias = _bias_table_call(P["rel_bias"], buckets)
    parts = []
    for g in range(N_GROUPS):
        parts += _attn_fwd_call(proj, bias, g)
    a, a_bf, lse = _attn_merge_call(parts)
    P = dict(P, **late_weights("mix", a_bf))
    y_a = _matmul(a_bf, P["w_attn_out"], "nn", "attn_out")
    c1 = _conv_fwd_call(proj, glu_col0, P["conv_dw_w"], P["conv_dw_b"])
    cact = _ln_silu_call(c1, P["conv_ln_g"], P["conv_ln_b"])
    y_c = _matmul(cact, P["conv_pw_w"], "nn", "conv_pw")
    mixed = _mix_call(proj, gate_col0, P["b_gate"], y_a, y_c)
    out = _matmul(mixed, P["w_out"], "nn", "mix_out")
    x1, h2 = _res1_call(xs, out, P["norm_mix_post"], P["norm_ffn_pre"])
    P = dict(P, **late_weights("ffn", h2))
    u = _matmul(h2, P["w_up"], "nn", "ffn_up")
    f = _ffn_fwd_call(u, P["ffn_conv_w"], P["ffn_conv_b"])
    yff = _matmul(f, P["w_down"], "nn", "ffn_down")
    loss_tile, dx2 = _loss_call(yff, x1, P["norm_ffn_post"], target)

    G = {}
    dyff, G["norm_ffn_post"] = _rms_bwd_call(yff, P["norm_ffn_post"], dx2, "rms_ffn_post_bwd")
    zero = on_grad("w_down", shard_major(_matmul(f, dyff, "tn", "ffn_down_dw")))
    df = _matmul(dyff, P["w_down"], "nt", "ffn_down_dx")
    dug, duv, dwg, dwv, dbg, dbv = _ffn_bwd_call(u, P["ffn_conv_w"], P["ffn_conv_b"] + zero, df)
    G["ffn_conv_w"] = jnp.concatenate([dwg[:FFN_CONV_WIDTH], dwv[:FFN_CONV_WIDTH]], axis=1)
    G["ffn_conv_b"] = jnp.concatenate([dbg, dbv], axis=1)
    du = jnp.concatenate([dug, duv], axis=1)
    zero = on_grad("w_up", _matmul(h2, du, "tn", "ffn_up_dw", out_shards=True))
    dh2 = _matmul(du, P["w_up"], "nt", "ffn_up_dx")
    dx1, dout, G["norm_ffn_pre"], G["norm_mix_post"] = _mid_bwd_call(x1, P["norm_ffn_pre"] + zero, dh2, dx2, out, P["norm_mix_post"])
    zero = on_grad("w_out", shard_major(_matmul(mixed, dout, "tn", "mix_out_dw")))
    dmixed = _matmul(dout, P["w_out"], "nt", "mix_out_dx")
    dya, dyc, dga, dgc, dba, dbc = _mix_bwd_call(dmixed, proj, gate_col0, P["b_gate"] + zero, y_a, y_c)
    G["b_gate"] = jnp.concatenate([dba, dbc], axis=1)
    zero = on_grad("w_attn_out", _matmul(a_bf, dya, "tn", "attn_out_dw", out_shards=True))
    zero = zero + on_grad("conv_pw_w", shard_major(_matmul(cact, dyc, "tn", "conv_pw_dw")))
    da = _matmul(dya, P["w_attn_out"], "nt", "attn_out_dx")
    dcact = _matmul(dyc, P["conv_pw_w"], "nt", "conv_pw_dx")
    dc1, G["conv_ln_g"], G["conv_ln_b"] = _ln_silu_bwd_call(c1, P["conv_ln_g"] + zero, P["conv_ln_b"], dcact)
    dval, dgate, dw_dw, G["conv_dw_b"] = _conv_bwd_call(proj, glu_col0, P["conv_dw_w"], dc1)
    G["conv_dw_w"] = dw_dw[:CONV_WIDTH]
    delta = _attn_delta_call(a, da)
    dqs, dks, dvs, dbs = [], [], [], []
    for g in range(N_GROUPS):
        dq, dk, dv, db = _attn_bwd_call(proj, bias, da, lse, delta, g)
        dqs.append(dq)
        dks.append(dk)
        dvs.append(dv)
        dbs.append(db)
    G["rel_bias"] = _bias_grad_call(jnp.concatenate(dbs, axis=0), buckets)
    dproj = _dproj_call(dqs + dks + dvs, [dval, dgate, dga, dgc])
    zero = on_grad("w_in", _matmul(h1, dproj, "tn", "proj_in_dw", out_shards=True, tm=512))
    dh1 = _matmul(dproj, P["w_in"], "nt", "proj_in_dx")
    grad_x, G["norm_mix_pre"] = _in_bwd_call(xs, P["norm_mix_pre"] + zero, dh1, dx1)
    return loss_tile, grad_x, G


def kernel(x, w_in, b_gate, rel_bias, w_attn_out, conv_dw_w, conv_dw_b, conv_ln_g, conv_ln_b, conv_pw_w, w_out, norm_mix_pre, norm_mix_post, norm_ffn_pre, norm_ffn_post, w_up, ffn_conv_w, ffn_conv_b, w_down, loss_target, m_w_in, m_b_gate, m_rel_bias, m_w_attn_out, m_conv_dw_w, m_conv_dw_b, m_conv_ln_g, m_conv_ln_b, m_conv_pw_w, m_w_out, m_norm_mix_pre, m_norm_mix_post, m_norm_ffn_pre, m_norm_ffn_post, m_w_up, m_ffn_conv_w, m_ffn_conv_b, m_w_down, v_w_in, v_b_gate, v_rel_bias, v_w_attn_out, v_conv_dw_w, v_conv_dw_b, v_conv_ln_g, v_conv_ln_b, v_conv_pw_w, v_w_out, v_norm_mix_pre, v_norm_mix_post, v_norm_ffn_pre, v_norm_ffn_post, v_w_up, v_ffn_conv_w, v_ffn_conv_b, v_w_down):
    weights = dict(w_in=w_in, b_gate=b_gate, rel_bias=rel_bias, w_attn_out=w_attn_out, conv_dw_w=conv_dw_w, conv_dw_b=conv_dw_b,
                   conv_ln_g=conv_ln_g, conv_ln_b=conv_ln_b, conv_pw_w=conv_pw_w, w_out=w_out, norm_mix_pre=norm_mix_pre,
                   norm_mix_post=norm_mix_post, norm_ffn_pre=norm_ffn_pre, norm_ffn_post=norm_ffn_post, w_up=w_up,
                   ffn_conv_w=ffn_conv_w, ffn_conv_b=ffn_conv_b, w_down=w_down)
    m_in = dict(w_in=m_w_in, b_gate=m_b_gate, rel_bias=m_rel_bias, w_attn_out=m_w_attn_out, conv_dw_w=m_conv_dw_w,
                conv_dw_b=m_conv_dw_b, conv_ln_g=m_conv_ln_g, conv_ln_b=m_conv_ln_b, conv_pw_w=m_conv_pw_w, w_out=m_w_out,
                norm_mix_pre=m_norm_mix_pre, norm_mix_post=m_norm_mix_post, norm_ffn_pre=m_norm_ffn_pre,
                norm_ffn_post=m_norm_ffn_post, w_up=m_w_up, ffn_conv_w=m_ffn_conv_w, ffn_conv_b=m_ffn_conv_b, w_down=m_w_down)
    v_in = dict(w_in=v_w_in, b_gate=v_b_gate, rel_bias=v_rel_bias, w_attn_out=v_w_attn_out, conv_dw_w=v_conv_dw_w,
                conv_dw_b=v_conv_dw_b, conv_ln_g=v_conv_ln_g, conv_ln_b=v_conv_ln_b, conv_pw_w=v_conv_pw_w, w_out=v_w_out,
                norm_mix_pre=v_norm_mix_pre, norm_mix_post=v_norm_mix_post, norm_ffn_pre=v_norm_ffn_pre,
                norm_ffn_post=v_norm_ffn_post, w_up=v_w_up, ffn_conv_w=v_ffn_conv_w, ffn_conv_b=v_ffn_conv_b, w_down=v_w_down)
    names = list(weights)
    xi, yi, ci = _position()
    chip = 2 * xi + yi
    core_arr = jnp.reshape(ci, (1,)).astype(jnp.int32)

    xs = x[0]
    target = loss_target[0]
    S, D = xs.shape

    big = ["w_in", "w_attn_out", "conv_pw_w", "w_out", "w_up", "w_down"]
    row_sharded = ("conv_pw_w", "w_out", "w_down")
    bf16_shard = {k: weights[k][0].astype(BF16) for k in big}
    natural = lambda k, g: g.reshape(-1, g.shape[2]) if k in row_sharded else g
    w_in_full, dw4, fc4 = _allgather_call([bf16_shard["w_in"]], [conv_dw_w[0], ffn_conv_w[0]])
    late_sets = dict(mix=["w_attn_out", "conv_pw_w", "w_out"], ffn=["w_up", "w_down"])
    started, after = {}, w_in_full
    for tag, keys in late_sets.items():
        srcs = [bf16_shard[k] for k in keys]
        lands = [lax.empty((N_CHIPS,) + s.shape, BF16) for s in srcs]
        started[tag] = _split_start(f"gather_{tag}_start", srcs, lands, 4 * len(keys), _gather_copies, after)
        after = started[tag]["tile"]
    launched = started["mix"]["token"] + started["ffn"]["token"]

    def late_weights(tag, after):
        landed = _split_wait(f"gather_{tag}_wait", started[tag], _gather_copies, after)[len(late_sets[tag]):]
        return {k: natural(k, g) for k, g in zip(late_sets[tag], landed)}

    chip_core = jnp.stack([chip, ci]).astype(jnp.int32)
    exchanging, pending = {}, {}

    def pair_up(after):
        token = jnp.float32(0.0)
        for k in list(exchanging):
            g3, r1 = _split_wait(f"sibling_exchange_wait_{k}", exchanging.pop(k), _sibling_copies, after)
            s32, s16 = _pair_sum_call(g3, r1, core_arr, f"pair_sum_{k}")
            land = lax.empty((3,) + s16.shape[1:], BF16)
            pending[k] = (s32, _split_start(f"chip_exchange_start_{k}", [s16], [land], 3, _exchange_copies, s32))
            token = token + pending[k][1]["token"]
        return token

    def on_grad(k, g3):
        token = pair_up(g3[0, :SUBLANES, :LANES])
        land = lax.empty((N_CHIPS, g3.shape[1] // 2, g3.shape[2]), F32)
        exchanging[k] = _split_start(f"sibling_exchange_start_{k}", [g3], [land], 1, _sibling_copies, core_arr)
        return token + exchanging[k]["token"]

    def finish(keys, after, tag):
        halves = []
        for k in keys:
            s32, st = pending[k]
            recv2 = _split_wait(f"chip_exchange_wait_{k}", st, _exchange_copies, after)[1]
            halves.append(_chip_sum_call(s32, recv2, chip_core, f"chip_sum_{k}"))
        return dict(zip(keys, _sibling_assemble_call(halves, f"grad_sibling_assemble_{tag}")))

    P = dict(w_in=w_in_full, conv_dw_w=jnp.concatenate(list(dw4), axis=1), ffn_conv_w=jnp.concatenate(list(fc4), axis=1),
             b_gate=b_gate, rel_bias=rel_bias, conv_dw_b=conv_dw_b, conv_ln_g=conv_ln_g, conv_ln_b=conv_ln_b,
             norm_mix_pre=norm_mix_pre + launched, norm_mix_post=norm_mix_post, norm_ffn_pre=norm_ffn_pre,
             norm_ffn_post=norm_ffn_post, ffn_conv_b=ffn_conv_b)
    loss_tile, grad_x, G = _local_step(xs, target, P, late_weights, on_grad)

    small = [k for k in names if k not in big]
    packed = _pack([loss_tile[:1]] + [G[k] for k in small])
    summed_block = _small_allreduce_call(packed)
    summed_block = summed_block + pair_up(summed_block[:SUBLANES])
    loss_row, *summed = _unpack(summed_block, [(1, LANES)] + [G[k].shape for k in small])
    loss = loss_row[0, 0]
    reduced = {}
    for k, gsum in zip(small, summed):
        if k in ("conv_dw_w", "ffn_conv_w"):
            cols = weights[k].shape[2]
            reduced[k] = lax.dynamic_slice_in_dim(gsum, chip * cols, cols, axis=1)
        else:
            reduced[k] = gsum

    grads, deltas, new_m, new_v = {}, {}, {}, {}

    def update(keys):
        for k in keys:
            d, mn, vn = _adamw_call(weights[k][0], reduced[k], m_in[k][0], v_in[k][0], f"adamw_{k}")
            grads[k], deltas[k], new_m[k], new_v[k] = reduced[k][None], d[None], mn[None], vn[None]

    others = [k for k in big if k != "w_in"]
    reduced.update(finish(others, summed_block, "others"))
    update(others)
    reduced.update(finish(["w_in"], deltas["w_up"], "w_in"))
    update(["w_in"])
    flat2 = lambda t: t.reshape(-1, t.shape[-1]) if t.ndim == 3 else t
    pw = _pack([flat2(weights[k]) for k in small])
    pg = _pack([reduced[k] for k in small])
    pm = _pack([flat2(m_in[k]) for k in small])
    pv = _pack([flat2(v_in[k]) for k in small])
    pd, pmn, pvn = _adamw_call(pw, pg, pm, pv, "adamw_small")
    shapes = [weights[k].shape for k in small]
    for k, gk, dk_, mk, vk in zip(small, [reduced[k] for k in small], _unpack(pd, shapes), _unpack(pmn, shapes), _unpack(pvn, shapes)):
        grads[k], deltas[k], new_m[k], new_v[k] = gk.reshape(weights[k].shape), dk_, mk, vk

    return (loss, grad_x[None], *[grads[k] for k in names], *[deltas[k] for k in names],
            *[new_m[k] for k in names], *[new_v[k] for k in names])
```

```python
import functools
import math

import jax
import jax.numpy as jnp
import numpy as np
from jax import lax
from jax.experimental import pallas as pl
from jax.experimental.pallas import tpu as pltpu

F32 = jnp.float32
BF16 = jnp.bfloat16
MESH = pl.DeviceIdType.MESH

HEAD_DIM = 128
HEADS_PER_GROUP = 4
DILATED_PATTERNS = ((128, 1), (512, 4), (2048, 16))
N_GROUPS = 3
N_HEADS = N_GROUPS * HEADS_PER_GROUP
SPAN = 128
GROUP_WIDTH = HEADS_PER_GROUP * HEAD_DIM
CONV_WIDTH = 31
FFN_CONV_WIDTH = 3
N_BUCKETS = 32
MAX_DISTANCE = 2048
RMS_EPS = 1e-6
LN_EPS = 1e-5
NEG_INF = -1e30
ADAM_LR = 0.001
ADAM_B1 = 0.9
ADAM_B2 = 0.999
ADAM_EPS = 1e-08
ADAM_WD = 0.01
ADAM_STEP = 10

LANES = 128
SUBLANES = 8
ROW_TILE = 256
TIME_BLOCK = 128
CONV_PAD = 32
FFN_PAD = 8
VMEM_LIMIT = 56 << 20


def _params(sem=None, vmem=None):
    kw = {}
    if sem is not None:
        kw["dimension_semantics"] = sem
    if vmem is not None:
        kw["vmem_limit_bytes"] = vmem
    return pltpu.CompilerParams(**kw)


def _pick(n, cands):
    for c in cands:
        if n % c == 0:
            return c
    return n


ELEMENTWISE_TILE_BYTES = 3 << 19


def _row_tile(rows, cols):
    for align in (16, SUBLANES):
        fits = [t for t in range(align, rows + 1, align) if rows % t == 0 and t * cols * 4 <= ELEMENTWISE_TILE_BYTES]
        if fits:
            return max(fits)
    return SUBLANES


N_CHIPS = 4
M_TILES = (1024, 1408, 512, 256, 128)
N_TILES = (512, 1408, 256, 128)
K_TILES = (2176, 2048, 1408, 1024, 512, 256, 128)


def _matmul(a, b, mode, name, out_shards=False, tm=None):
    assert a.dtype == BF16 and b.dtype == BF16, (name, a.dtype, b.dtype)
    b3 = b.ndim == 3
    tn = tk = None
    if mode == "nn":
        M, K = a.shape
        N = b.shape[-1] * (N_CHIPS if b3 else 1)
        tn = b.shape[-1] if b3 else None
    elif mode == "nt":
        M, K = a.shape
        N = b.shape[-2]
        tk = b.shape[-1] if b3 else None
    else:
        K, M = a.shape
        N = b.shape[1]
        tn = N // N_CHIPS if out_shards else None
    tm = tm or _pick(M, M_TILES)
    tn = tn or _pick(N, N_TILES)
    tk = tk or _pick(K, K_TILES)
    nk = K // tk
    dn = {"nn": (((1,), (0,)), ((), ())), "nt": (((1,), (1,)), ((), ())), "tn": (((0,), (0,)), ((), ()))}[mode]

    def body(a_ref, b_ref, o_ref):
        if nk == 1:
            o_ref[...] = lax.dot_general(a_ref[...], b_ref[...], dn, preferred_element_type=F32)
        else:
            @pl.when(pl.program_id(2) == 0)
            def _():
                o_ref[...] = jnp.zeros_like(o_ref)

            o_ref[...] += lax.dot_general(a_ref[...], b_ref[...], dn, preferred_element_type=F32)

    if mode == "tn":
        a_spec = pl.BlockSpec((tk, tm), lambda i, j, k: (k, i))
    else:
        a_spec = pl.BlockSpec((tm, tk), lambda i, j, k: (i, k))
    if mode == "nn":
        b_spec = pl.BlockSpec((None, tk, tn), lambda i, j, k: (j, k, 0)) if b3 else pl.BlockSpec((tk, tn), lambda i, j, k: (k, j))
    elif mode == "nt":
        b_spec = pl.BlockSpec((None, tn, tk), lambda i, j, k: (k, j, 0)) if b3 else pl.BlockSpec((tn, tk), lambda i, j, k: (j, k))
    else:
        b_spec = pl.BlockSpec((tk, tn), lambda i, j, k: (k, j))
    if out_shards:
        out_spec = pl.BlockSpec((None, tm, tn), lambda i, j, k: (j, i, 0))
        out_shape = jax.ShapeDtypeStruct((N_CHIPS, M, tn), F32)
    else:
        out_spec = pl.BlockSpec((tm, tn), lambda i, j, k: (i, j))
        out_shape = jax.ShapeDtypeStruct((M, N), F32)
    return pl.pallas_call(
        body, name=name, grid=(M // tm, N // tn, nk),
        in_specs=[a_spec, b_spec], out_specs=out_spec, out_shape=out_shape,
        compiler_params=_params(("parallel", "parallel", "arbitrary"), VMEM_LIMIT),
    )(a, b)


def _rms(x, g):
    r = lax.rsqrt(jnp.mean(x * x, axis=-1, keepdims=True) + RMS_EPS)
    return x * r * g


def _rms_bwd(x, g, dy):
    r = lax.rsqrt(jnp.mean(x * x, axis=-1, keepdims=True) + RMS_EPS)
    n = x * r
    dn = dy * g
    dx = r * (dn - n * jnp.mean(dn * n, axis=-1, keepdims=True))
    return dx, jnp.sum(dy * n, axis=0, keepdims=True)


def _sigmoid(x):
    return 1.0 / (1.0 + jnp.exp(-x))


_GELU_C = math.sqrt(2.0 / math.pi)


def _gelu(x):
    return 0.5 * x * (1.0 + jnp.tanh(_GELU_C * (x + 0.044715 * x * x * x)))


def _gelu_grad(x):
    t = jnp.tanh(_GELU_C * (x + 0.044715 * x * x * x))
    return 0.5 * (1.0 + t) + 0.5 * x * (1.0 - t * t) * _GELU_C * (1.0 + 3.0 * 0.044715 * x * x)


def _row_spec(width, col_block=0):
    return pl.BlockSpec((ROW_TILE, width), lambda i: (i, col_block))


def _vec_spec(width, col_block=0):
    return pl.BlockSpec((1, width), lambda i: (0, col_block))


def _accumulate(ref, part):
    @pl.when(pl.program_id(0) == 0)
    def _():
        ref[...] = part

    @pl.when(pl.program_id(0) > 0)
    def _():
        ref[...] += part


def _rms_fwd_call(x, g):
    S, D = x.shape

    def body(x_ref, g_ref, h_ref):
        h_ref[...] = _rms(x_ref[...], g_ref[...]).astype(BF16)

    return pl.pallas_call(
        body, name="rms_mix_pre", grid=(S // ROW_TILE,),
        in_specs=[_row_spec(D), _vec_spec(D)], out_specs=_row_spec(D),
        out_shape=jax.ShapeDtypeStruct((S, D), BF16),
        compiler_params=_params(("parallel",)),
    )(x, g)


def _ln_silu_call(c1, g, b):
    S, C = c1.shape

    def body(c_ref, g_ref, b_ref, o_ref):
        xv = c_ref[...]
        mu = jnp.mean(xv, axis=-1, keepdims=True)
        xc = xv - mu
        var = jnp.mean(xc * xc, axis=-1, keepdims=True)
        z = xc * lax.rsqrt(var + LN_EPS) * g_ref[...] + b_ref[...]
        o_ref[...] = (z * _sigmoid(z)).astype(BF16)

    return pl.pallas_call(
        body, name="conv_ln_silu", grid=(S // ROW_TILE,),
        in_specs=[_row_spec(C), _vec_spec(C), _vec_spec(C)], out_specs=_row_spec(C),
        out_shape=jax.ShapeDtypeStruct((S, C), BF16),
        compiler_params=_params(("parallel",)),
    )(c1, g, b)


def _ln_silu_bwd_call(c1, g, b, dc):
    S, C = c1.shape

    def body(c_ref, g_ref, b_ref, dc_ref, dx_ref, dg_ref, db_ref):
        xv = c_ref[...]
        mu = jnp.mean(xv, axis=-1, keepdims=True)
        xc = xv - mu
        rs = lax.rsqrt(jnp.mean(xc * xc, axis=-1, keepdims=True) + LN_EPS)
        xh = xc * rs
        z = xh * g_ref[...] + b_ref[...]
        sg = _sigmoid(z)
        dz = dc_ref[...] * (sg * (1.0 + z * (1.0 - sg)))
        dxh = dz * g_ref[...]
        dx_ref[...] = rs * (dxh - jnp.mean(dxh, axis=-1, keepdims=True) - xh * jnp.mean(dxh * xh, axis=-1, keepdims=True))
        _accumulate(dg_ref, jnp.sum(dz * xh, axis=0, keepdims=True))
        _accumulate(db_ref, jnp.sum(dz, axis=0, keepdims=True))

    return pl.pallas_call(
        body, name="conv_ln_silu_bwd", grid=(S // ROW_TILE,),
        in_specs=[_row_spec(C), _vec_spec(C), _vec_spec(C), _row_spec(C)],
        out_specs=[_row_spec(C), _vec_spec(C), _vec_spec(C)],
        out_shape=[jax.ShapeDtypeStruct((S, C), F32), jax.ShapeDtypeStruct((1, C), F32), jax.ShapeDtypeStruct((1, C), F32)],
        compiler_params=_params(("arbitrary",)),
    )(c1, g, b, dc)


def _mix_call(proj, gate_col0, b_gate, y_a, y_c):
    S, D = y_a.shape
    w = 512
    nc = D // w
    ga0, gc0 = gate_col0 // w, (gate_col0 + D) // w

    def body(ga_ref, gc_ref, ba_ref, bc_ref, ya_ref, yc_ref, o_ref):
        o_ref[...] = (_sigmoid(ga_ref[...] + ba_ref[...]) * ya_ref[...]
                      + _sigmoid(gc_ref[...] + bc_ref[...]) * yc_ref[...]).astype(BF16)

    tile = lambda off: pl.BlockSpec((ROW_TILE, w), lambda i, j: (i, off + j))
    vec = lambda off: pl.BlockSpec((1, w), lambda i, j: (0, off + j))
    return pl.pallas_call(
        body, name="gate_mix", grid=(S // ROW_TILE, nc),
        in_specs=[tile(ga0), tile(gc0), vec(0), vec(nc), tile(0), tile(0)],
        out_specs=tile(0), out_shape=jax.ShapeDtypeStruct((S, D), BF16),
        compiler_params=_params(("parallel", "parallel")),
    )(proj, proj, b_gate, b_gate, y_a, y_c)


def _mix_bwd_call(dmixed, proj, gate_col0, b_gate, y_a, y_c):
    S, D = y_a.shape
    w = 512
    nc = D // w
    ga0, gc0 = gate_col0 // w, (gate_col0 + D) // w

    def body(dm_ref, ga_ref, gc_ref, ba_ref, bc_ref, ya_ref, yc_ref, dya_ref, dyc_ref, dga_ref, dgc_ref, dba_ref, dbc_ref):
        dm = dm_ref[...]
        sa = _sigmoid(ga_ref[...] + ba_ref[...])
        sc = _sigmoid(gc_ref[...] + bc_ref[...])
        dya_ref[...] = (dm * sa).astype(BF16)
        dyc_ref[...] = (dm * sc).astype(BF16)
        dga = dm * ya_ref[...] * sa * (1.0 - sa)
        dgc = dm * yc_ref[...] * sc * (1.0 - sc)
        dga_ref[...] = dga.astype(BF16)
        dgc_ref[...] = dgc.astype(BF16)
        pa = jnp.sum(dga, axis=0, keepdims=True)
        pc = jnp.sum(dgc, axis=0, keepdims=True)

        @pl.when(pl.program_id(1) == 0)
        def _():
            dba_ref[...] = pa
            dbc_ref[...] = pc

        @pl.when(pl.program_id(1) > 0)
        def _():
            dba_ref[...] += pa
            dbc_ref[...] += pc

    tile = lambda off: pl.BlockSpec((ROW_TILE, w), lambda j, i: (i, off + j))
    vec = lambda off: pl.BlockSpec((1, w), lambda j, i: (0, off + j))
    return pl.pallas_call(
        body, name="gate_mix_bwd", grid=(nc, S // ROW_TILE),
        in_specs=[tile(0), tile(ga0), tile(gc0), vec(0), vec(nc), tile(0), tile(0)],
        out_specs=[tile(0), tile(0), tile(0), tile(0), vec(0), vec(0)],
        out_shape=[jax.ShapeDtypeStruct((S, D), BF16)] * 4 + [
                   jax.ShapeDtypeStruct((1, D), F32), jax.ShapeDtypeStruct((1, D), F32)],
        compiler_params=_params(("parallel", "arbitrary")),
    )(dmixed, proj, proj, b_gate, b_gate, y_a, y_c)


def _res1_call(x, out, g_post, g_pre):
    S, D = x.shape

    def body(x_ref, o_ref, gp_ref, gq_ref, x1_ref, h2_ref):
        x1 = x_ref[...] + _rms(o_ref[...], gp_ref[...])
        x1_ref[...] = x1
        h2_ref[...] = _rms(x1, gq_ref[...]).astype(BF16)

    return pl.pallas_call(
        body, name="residual_mix", grid=(S // ROW_TILE,),
        in_specs=[_row_spec(D), _row_spec(D), _vec_spec(D), _vec_spec(D)],
        out_specs=[_row_spec(D), _row_spec(D)],
        out_shape=[jax.ShapeDtypeStruct((S, D), F32), jax.ShapeDtypeStruct((S, D), BF16)],
        compiler_params=_params(("parallel",)),
    )(x, out, g_post, g_pre)


def _loss_call(y, x1, g_post, target):
    S, D = y.shape

    def body(y_ref, x1_ref, g_ref, t_ref, loss_ref, dx_ref):
        err = x1_ref[...] + _rms(y_ref[...], g_ref[...]) - t_ref[...]
        dx_ref[...] = err * (1.0 / D)
        part = 0.5 * jnp.sum(jnp.mean(err * err, axis=-1, keepdims=True), axis=0, keepdims=True)
        _accumulate(loss_ref, jnp.broadcast_to(part, (SUBLANES, LANES)))

    return pl.pallas_call(
        body, name="residual_ffn_loss", grid=(S // ROW_TILE,),
        in_specs=[_row_spec(D), _row_spec(D), _vec_spec(D), _row_spec(D)],
        out_specs=[pl.BlockSpec((SUBLANES, LANES), lambda i: (0, 0)), _row_spec(D)],
        out_shape=[jax.ShapeDtypeStruct((SUBLANES, LANES), F32), jax.ShapeDtypeStruct((S, D), F32)],
        compiler_params=_params(("arbitrary",)),
    )(y, x1, g_post, target)


def _rms_bwd_call(x, g, dy, name):
    S, D = x.shape

    def body(x_ref, g_ref, dy_ref, dx_ref, dg_ref):
        dx, dg = _rms_bwd(x_ref[...], g_ref[...], dy_ref[...])
        dx_ref[...] = dx.astype(BF16)
        _accumulate(dg_ref, dg)

    return pl.pallas_call(
        body, name=name, grid=(S // ROW_TILE,),
        in_specs=[_row_spec(D), _vec_spec(D), _row_spec(D)],
        out_specs=[_row_spec(D), _vec_spec(D)],
        out_shape=[jax.ShapeDtypeStruct((S, D), BF16), jax.ShapeDtypeStruct((1, D), F32)],
        compiler_params=_params(("arbitrary",)),
    )(x, g, dy)


def _mid_bwd_call(x1, g_pre, dh2, dx2, out, g_post):
    S, D = x1.shape

    def body(x1_ref, gq_ref, dh_ref, dx2_ref, o_ref, gp_ref, dx1_ref, do_ref, dgq_ref, dgp_ref):
        d, dgq = _rms_bwd(x1_ref[...], gq_ref[...], dh_ref[...])
        dx1 = dx2_ref[...] + d
        dx1_ref[...] = dx1
        do, dgp = _rms_bwd(o_ref[...], gp_ref[...], dx1)
        do_ref[...] = do.astype(BF16)
        _accumulate(dgq_ref, dgq)
        _accumulate(dgp_ref, dgp)

    return pl.pallas_call(
        body, name="residual_mix_bwd", grid=(S // ROW_TILE,),
        in_specs=[_row_spec(D), _vec_spec(D), _row_spec(D), _row_spec(D), _row_spec(D), _vec_spec(D)],
        out_specs=[_row_spec(D), _row_spec(D), _vec_spec(D), _vec_spec(D)],
        out_shape=[jax.ShapeDtypeStruct((S, D), F32), jax.ShapeDtypeStruct((S, D), BF16)] + [jax.ShapeDtypeStruct((1, D), F32)] * 2,
        compiler_params=_params(("arbitrary",)),
    )(x1, g_pre, dh2, dx2, out, g_post)


def _in_bwd_call(x, g, dh1, dx1):
    S, D = x.shape

    def body(x_ref, g_ref, dh_ref, dx1_ref, gx_ref, dg_ref):
        d, dg = _rms_bwd(x_ref[...], g_ref[...], dh_ref[...])
        gx_ref[...] = dx1_ref[...] + d
        _accumulate(dg_ref, dg)

    return pl.pallas_call(
        body, name="rms_mix_pre_bwd", grid=(S // ROW_TILE,),
        in_specs=[_row_spec(D), _vec_spec(D), _row_spec(D), _row_spec(D)],
        out_specs=[_row_spec(D), _vec_spec(D)],
        out_shape=[jax.ShapeDtypeStruct((S, D), F32), jax.ShapeDtypeStruct((1, D), F32)],
        compiler_params=_params(("arbitrary",)),
    )(x, g, dh1, dx1)


def _bucket_table(dilation):
    qi = np.arange(SPAN)[:, None]
    ki = np.arange(2 * SPAN)[None, :]
    dist = np.maximum(qi + SPAN - ki, 0) * dilation
    max_exact = N_BUCKETS // 2
    d = np.maximum(dist, 1).astype(np.float64)
    large = max_exact + (np.log(d / max_exact) / math.log(MAX_DISTANCE / max_exact) * (N_BUCKETS - max_exact)).astype(np.int32)
    large = np.minimum(large, N_BUCKETS - 1)
    return np.where(dist < max_exact, dist, large).astype(np.int32)


def _bucket_tables():
    return jnp.asarray(np.stack([_bucket_table(r) for _, r in DILATED_PATTERNS]))


def _bias_table_call(rel_bias, buckets):
    def body(rb_ref, bk_ref, o_ref):
        for h in range(N_HEADS):
            bk = bk_ref[h // HEADS_PER_GROUP]

            def step(b, acc):
                return jnp.where(bk == b, rb_ref[b, h], acc)

            o_ref[h] = lax.fori_loop(0, N_BUCKETS, step, jnp.zeros((SPAN, 2 * SPAN), F32))

    return pl.pallas_call(
        body, name="rel_bias_table",
        in_specs=[pl.BlockSpec(memory_space=pltpu.SMEM), pl.BlockSpec(memory_space=pltpu.VMEM)],
        out_specs=pl.BlockSpec(memory_space=pltpu.VMEM),
        out_shape=jax.ShapeDtypeStruct((N_HEADS, SPAN, 2 * SPAN), F32),
    )(rel_bias, buckets)


def _bias_grad_call(dbias, buckets):
    def body(db_ref, bk_ref, o_ref, rows_ref):
        for h in range(N_HEADS):
            bk = bk_ref[h // HEADS_PER_GROUP]
            dv = db_ref[h]

            def step(b, carry):
                rows_ref[h, b] = jnp.sum(jnp.where(bk == b, dv, 0.0), axis=0, keepdims=True)
                return carry

            lax.fori_loop(0, N_BUCKETS, step, 0)
        o_ref[...] = jnp.sum(rows_ref[...], axis=-1, keepdims=True)

    out = pl.pallas_call(
        body, name="rel_bias_grad",
        in_specs=[pl.BlockSpec(memory_space=pltpu.VMEM), pl.BlockSpec(memory_space=pltpu.VMEM)],
        out_specs=pl.BlockSpec(memory_space=pltpu.VMEM),
        out_shape=jax.ShapeDtypeStruct((N_HEADS, N_BUCKETS, 1, 1), F32),
        scratch_shapes=[pltpu.VMEM((N_HEADS, N_BUCKETS, 1, 2 * SPAN), F32)],
    )(dbias, buckets)
    return out.reshape(N_HEADS, N_BUCKETS).T


def _dot_nt(a, b):
    return lax.dot_general(a, b, (((1,), (1,)), ((), ())), preferred_element_type=F32)


def _dot_nn(a, b):
    return lax.dot_general(a, b, (((1,), (0,)), ((), ())), preferred_element_type=F32)


def _dot_tn(a, b):
    return lax.dot_general(a, b, (((0,), (0,)), ((), ())), preferred_element_type=F32)


def _band_masks(n, nb):
    qi = lax.broadcasted_iota(jnp.int32, (SPAN, SPAN), 0)
    ki = lax.broadcasted_iota(jnp.int32, (SPAN, SPAN), 1)
    prev_ok = jnp.logical_and(ki >= qi, n > 0)
    cur_ok = ki <= qi
    next_ok = jnp.logical_and(ki >= qi, n < nb - 1)
    return prev_ok, cur_ok, next_ok


def _attn_plan(S, group):
    r = DILATED_PATTERNS[group][1]
    hp, per = (HEADS_PER_GROUP, 1) if r == 1 else (2, 4)
    return r, S // (r * SPAN), hp, per


def _residue_rows(rho, r):
    return slice(None) if r == 1 else pl.ds(rho, SPAN, stride=r)


def _for_residues(r, per, fn):
    if r == per:
        for u in range(per):
            fn(u)
        return

    def step(i, carry):
        for u in range(per):
            fn(i * per + u)
        return carry

    lax.fori_loop(0, r // per, step, 0)


def _attn_fwd_call(proj, bias, group):
    S = proj.shape[0]
    r, nb, hp, per = _attn_plan(S, group)
    scale = HEAD_DIM ** -0.5
    kinds = ("q", "kp", "kc", "vp", "vc") if nb > 1 else ("q", "kc", "vc")

    def body(*refs):
        ins = {kind: refs[i * hp:(i + 1) * hp] for i, kind in enumerate(kinds)}
        b_ref, o_ref, lse_ref = refs[len(kinds) * hp:]
        n = pl.program_id(1)
        prev_ok, cur_ok, _ = _band_masks(n, nb)

        def residue(rho):
            rows = _residue_rows(rho, r)
            for j in range(hp):
                q = ins["q"][j][rows, :].astype(BF16)
                sc = jnp.where(cur_ok, _dot_nt(q, ins["kc"][j][rows, :].astype(BF16)) * scale + b_ref[j, :, SPAN:], NEG_INF)
                m = jnp.max(sc, axis=-1, keepdims=True)
                if nb > 1:
                    sp = jnp.where(prev_ok, _dot_nt(q, ins["kp"][j][rows, :].astype(BF16)) * scale + b_ref[j, :, :SPAN], NEG_INF)
                    m = jnp.maximum(m, jnp.max(sp, axis=-1, keepdims=True))
                pc = jnp.exp(sc - m)
                den = jnp.sum(pc, axis=-1, keepdims=True)
                acc = _dot_nn(pc.astype(BF16), ins["vc"][j][rows, :].astype(BF16))
                if nb > 1:
                    pp = jnp.exp(sp - m)
                    den = den + jnp.sum(pp, axis=-1, keepdims=True)
                    acc = acc + _dot_nn(pp.astype(BF16), ins["vp"][j][rows, :].astype(BF16))
                o_ref[j, rows, :] = acc / den
                lse_ref[j, rows, :] = jnp.broadcast_to(m + jnp.log(den), (SPAN, HEAD_DIM))

        _for_residues(r, per, residue)

    in_specs = [_head_spec(r, nb, hp, kind, group, jj) for kind in kinds for jj in range(hp)]
    in_specs.append(pl.BlockSpec((hp, SPAN, 2 * SPAN), lambda j, n: (group * (HEADS_PER_GROUP // hp) + j, 0, 0)))
    out = pl.BlockSpec((hp, r * SPAN, HEAD_DIM), lambda j, n: (j, n, 0))
    return pl.pallas_call(
        body, name=f"attn_fwd_g{group}", grid=(HEADS_PER_GROUP // hp, nb),
        in_specs=in_specs, out_specs=[out] * 2,
        out_shape=[jax.ShapeDtypeStruct((HEADS_PER_GROUP, S, HEAD_DIM), F32)] * 2,
        compiler_params=_params(("parallel", "parallel"), VMEM_LIMIT),
    )(*([proj] * (len(in_specs) - 1)), bias)


_PROJ_PART = dict(q=0, qn=0, kp=1, kc=1, vp=2, vc=2)


def _head_spec(r, nb, hp, kind, group, jj):
    if kind in _PROJ_PART:
        base = (_PROJ_PART[kind] * N_GROUPS + group) * HEADS_PER_GROUP
    else:
        base = 0
    if kind.endswith("p"):
        row = lambda n: jnp.maximum(n - 1, 0)
    elif kind.endswith("n"):
        row = lambda n: jnp.minimum(n + 1, nb - 1)
    else:
        row = lambda n: n
    return pl.BlockSpec((r * SPAN, HEAD_DIM), lambda j, n: (row(n), base + j * hp + jj))


def _attn_merge_call(parts):
    S = parts[0].shape[1]

    def body(o1, s1, o2, s2, o3, s3, a_ref, ab_ref, lse_ref):
        for j in range(HEADS_PER_GROUP):
            sl = slice(j * HEAD_DIM, (j + 1) * HEAD_DIM)
            mx = jnp.maximum(jnp.maximum(s1[j], s2[j]), s3[j])
            w1 = jnp.exp(s1[j] - mx)
            w2 = jnp.exp(s2[j] - mx)
            w3 = jnp.exp(s3[j] - mx)
            den = w1 + w2 + w3
            a = (w1 * o1[j] + w2 * o2[j] + w3 * o3[j]) / den
            a_ref[:, sl] = a
            ab_ref[:, sl] = a.astype(BF16)
            lse_ref[:, sl] = mx + jnp.log(den)

    heads = pl.BlockSpec((HEADS_PER_GROUP, ROW_TILE, HEAD_DIM), lambda i: (0, i, 0))
    return pl.pallas_call(
        body, name="attn_merge", grid=(S // ROW_TILE,),
        in_specs=[heads] * 6, out_specs=[_row_spec(GROUP_WIDTH)] * 3,
        out_shape=[jax.ShapeDtypeStruct((S, GROUP_WIDTH), F32), jax.ShapeDtypeStruct((S, GROUP_WIDTH), BF16),
                   jax.ShapeDtypeStruct((S, GROUP_WIDTH), F32)],
        compiler_params=_params(("parallel",)),
    )(*parts)


def _attn_delta_call(a, da):
    S = a.shape[0]

    def body(a_ref, da_ref, d_ref):
        for j in range(HEADS_PER_GROUP):
            sl = slice(j * HEAD_DIM, (j + 1) * HEAD_DIM)
            d = jnp.sum(a_ref[:, sl] * da_ref[:, sl], axis=-1, keepdims=True)
            d_ref[:, sl] = jnp.broadcast_to(d, (ROW_TILE, HEAD_DIM))

    return pl.pallas_call(
        body, name="attn_delta", grid=(S // ROW_TILE,),
        in_specs=[_row_spec(GROUP_WIDTH)] * 2, out_specs=_row_spec(GROUP_WIDTH),
        out_shape=jax.ShapeDtypeStruct((S, GROUP_WIDTH), F32),
        compiler_params=_params(("parallel",)),
    )(a, da)


def _attn_bwd_call(proj, bias, da, lse, delta, group):
    S = proj.shape[0]
    r, nb, hp, per = _attn_plan(S, group)
    scale = HEAD_DIM ** -0.5
    kinds = ("q", "qn", "kp", "kc", "vp", "vc", "da", "dan", "lse", "lsen", "dl", "dln") if nb > 1 else ("q", "kc", "vc", "da", "lse", "dl")
    source = dict(da=da, dan=da, lse=lse, lsen=lse, dl=delta, dln=delta)

    def body(*refs):
        ins = {kind: refs[i * hp:(i + 1) * hp] for i, kind in enumerate(kinds)}
        b_ref, dq_ref, dk_ref, dv_ref, db_ref = refs[len(kinds) * hp:]
        n = pl.program_id(1)
        prev_ok, cur_ok, next_ok = _band_masks(n, nb)

        @pl.when(n == 0)
        def _():
            db_ref[...] = jnp.zeros_like(db_ref)

        def residue(rho):
            rows = _residue_rows(rho, r)
            for j in range(hp):
                get = lambda kind: ins[kind][j][rows, :]
                q = get("q").astype(BF16)
                kc = get("kc").astype(BF16)
                vc = get("vc").astype(BF16)
                dav = get("da").astype(BF16)
                lse_q, dl_q = get("lse"), get("dl")
                pc = jnp.exp(jnp.where(cur_ok, _dot_nt(q, kc) * scale + b_ref[j, :, SPAN:], NEG_INF) - lse_q)
                dsc = pc * (_dot_nt(dav, vc) - dl_q)
                dsc_b = dsc.astype(BF16)
                dq = _dot_nn(dsc_b, kc)
                dk = _dot_tn(dsc_b, q)
                dv = _dot_tn(pc.astype(BF16), dav)
                db_ref[j, :, SPAN:] += dsc
                if nb > 1:
                    kp = get("kp").astype(BF16)
                    vp = get("vp").astype(BF16)
                    qn = get("qn").astype(BF16)
                    dan = get("dan").astype(BF16)
                    bp = b_ref[j, :, :SPAN]
                    pp = jnp.exp(jnp.where(prev_ok, _dot_nt(q, kp) * scale + bp, NEG_INF) - lse_q)
                    dsp = pp * (_dot_nt(dav, vp) - dl_q)
                    dq = dq + _dot_nn(dsp.astype(BF16), kp)
                    db_ref[j, :, :SPAN] += dsp
                    pn = jnp.exp(jnp.where(next_ok, _dot_nt(qn, kc) * scale + bp, NEG_INF) - get("lsen"))
                    dsn = pn * (_dot_nt(dan, vc) - get("dln"))
                    dk = dk + _dot_tn(dsn.astype(BF16), qn)
                    dv = dv + _dot_tn(pn.astype(BF16), dan)
                dq_ref[j, rows, :] = dq * scale
                dk_ref[j, rows, :] = dk * scale
                dv_ref[j, rows, :] = dv

        _for_residues(r, per, residue)

    per_group = HEADS_PER_GROUP // hp
    band = (hp, SPAN, 2 * SPAN)
    in_specs = [_head_spec(r, nb, hp, kind, group, jj) for kind in kinds for jj in range(hp)]
    in_specs.append(pl.BlockSpec(band, lambda j, n: (group * per_group + j, 0, 0)))
    operands = [source.get(kind, proj) for kind in kinds for _ in range(hp)] + [bias]
    out = pl.BlockSpec((hp, r * SPAN, HEAD_DIM), lambda j, n: (j, n, 0))
    return pl.pallas_call(
        body, name=f"attn_bwd_g{group}", grid=(per_group, nb),
        in_specs=in_specs,
        out_specs=[out] * 3 + [pl.BlockSpec(band, lambda j, n: (j, 0, 0))],
        out_shape=[jax.ShapeDtypeStruct((HEADS_PER_GROUP, S, HEAD_DIM), F32)] * 3
        + [jax.ShapeDtypeStruct((HEADS_PER_GROUP, SPAN, 2 * SPAN), F32)],
        compiler_params=_params(("parallel", "arbitrary"), VMEM_LIMIT),
    )(*operands)


def _dproj_call(dqkv, tails):
    S = tails[0].shape[0]
    width = len(dqkv) * GROUP_WIDTH + sum(t.shape[1] for t in tails)

    def body(*refs):
        o_ref = refs[-1]
        col = 0
        for ref in refs[:len(dqkv)]:
            for j in range(HEADS_PER_GROUP):
                o_ref[:, col:col + HEAD_DIM] = ref[j].astype(BF16)
                col += HEAD_DIM
        for ref in refs[len(dqkv):-1]:
            o_ref[:, col:col + ref.shape[1]] = ref[...]
            col += ref.shape[1]

    heads = pl.BlockSpec((HEADS_PER_GROUP, ROW_TILE, HEAD_DIM), lambda i: (0, i, 0))
    return pl.pallas_call(
        body, name="dproj_assemble", grid=(S // ROW_TILE,),
        in_specs=[heads] * len(dqkv) + [_row_spec(t.shape[1]) for t in tails],
        out_specs=_row_spec(width), out_shape=jax.ShapeDtypeStruct((S, width), BF16),
        compiler_params=_params(("parallel",)),
    )(*dqkv, *tails)


def _taps(width):
    return [(k, (width - 1 - k) // SUBLANES, (width - 1 - k) % SUBLANES) for k in range(width)]


def _shifted(win, width, pad, up):
    total = win.shape[0]
    for b in range(SUBLANES):
        taps = [(k, a) for k, a, bb in _taps(width) if bb == b]
        if not taps:
            continue
        if up:
            rolled = win if b == 0 else pltpu.roll(win, total - b, axis=0)
        else:
            rolled = win if b == 0 else pltpu.roll(win, b, axis=0)
        for k, a in taps:
            start = SUBLANES * a if up else pad - SUBLANES * a
            yield k, rolled[start:start + TIME_BLOCK, :]


def _conv_block(win, w_ref, width, pad):
    acc = None
    for k, rows in _shifted(win, width, pad, up=False):
        term = w_ref[k:k + 1, :] * rows
        acc = term if acc is None else acc + term
    return acc


def _conv_transpose_block(win, w_ref, width, pad):
    acc = None
    for k, rows in _shifted(win, width, pad, up=True):
        term = w_ref[k:k + 1, :] * rows
        acc = term if acc is None else acc + term
    return acc


def _conv_weight_grad(win, dy, dw_ref, width, pad):
    for k, rows in _shifted(win, width, pad, up=False):
        dw_ref[k:k + 1, :] += jnp.sum(dy * rows, axis=0, keepdims=True)


def _time_loop(S, step):
    def it(tb, carry):
        step(pl.multiple_of(tb * TIME_BLOCK, TIME_BLOCK))
        return carry

    lax.fori_loop(0, S // TIME_BLOCK, it, 0)


def _conv_fwd_call(proj, col0, w, b):
    S = proj.shape[0]
    C = w.shape[1]
    nt = C // LANES
    v0, g0 = col0 // LANES, (col0 + C) // LANES

    def body(val_ref, gate_ref, w_ref, b_ref, o_ref, pad_ref):
        pad_ref[0:CONV_PAD, :] = jnp.zeros((CONV_PAD, LANES), F32)
        pad_ref[CONV_PAD:, :] = val_ref[...] * _sigmoid(gate_ref[...])

        def step(t0):
            win = pad_ref[pl.ds(t0, TIME_BLOCK + CONV_PAD), :]
            o_ref[pl.ds(t0, TIME_BLOCK), :] = _conv_block(win, w_ref, CONV_WIDTH, CONV_PAD) + b_ref[...]

        _time_loop(S, step)

    seq = lambda off: pl.BlockSpec((S, LANES), lambda i: (0, off + i))
    return pl.pallas_call(
        body, name="conv_module", grid=(nt,),
        in_specs=[seq(v0), seq(g0), pl.BlockSpec((CONV_WIDTH, LANES), lambda i: (0, i)), pl.BlockSpec((1, LANES), lambda i: (0, i))],
        out_specs=seq(0), out_shape=jax.ShapeDtypeStruct((S, C), F32),
        scratch_shapes=[pltpu.VMEM((S + CONV_PAD, LANES), F32)],
        compiler_params=_params(("parallel",)),
    )(proj, proj, w, b)


def _conv_bwd_call(proj, col0, w, dc1):
    S = proj.shape[0]
    C = w.shape[1]
    nt = C // LANES
    v0, g0 = col0 // LANES, (col0 + C) // LANES

    def body(val_ref, gate_ref, w_ref, dy_ref, dval_ref, dgate_ref, dw_ref, db_ref, xpad_ref, dpad_ref, dwacc_ref):
        xpad_ref[0:CONV_PAD, :] = jnp.zeros((CONV_PAD, LANES), F32)
        xpad_ref[CONV_PAD:, :] = val_ref[...] * _sigmoid(gate_ref[...])
        dpad_ref[0:S, :] = dy_ref[...]
        dpad_ref[S:, :] = jnp.zeros((CONV_PAD, LANES), F32)
        dwacc_ref[...] = jnp.zeros_like(dwacc_ref)

        def step(t0):
            rows = pl.ds(t0, TIME_BLOCK)
            _conv_weight_grad(xpad_ref[pl.ds(t0, TIME_BLOCK + CONV_PAD), :], dy_ref[rows, :], dwacc_ref, CONV_WIDTH, CONV_PAD)
            dc0 = _conv_transpose_block(dpad_ref[pl.ds(t0, TIME_BLOCK + CONV_PAD), :], w_ref, CONV_WIDTH, CONV_PAD)
            sg = _sigmoid(gate_ref[rows, :])
            dval_ref[rows, :] = (dc0 * sg).astype(BF16)
            dgate_ref[rows, :] = (dc0 * val_ref[rows, :] * sg * (1.0 - sg)).astype(BF16)

        _time_loop(S, step)
        dw_ref[...] = dwacc_ref[...]
        db_ref[...] = jnp.sum(dy_ref[...], axis=0, keepdims=True)

    seq = lambda off: pl.BlockSpec((S, LANES), lambda i: (0, off + i))
    return pl.pallas_call(
        body, name="conv_module_bwd", grid=(nt,),
        in_specs=[seq(v0), seq(g0), pl.BlockSpec((CONV_WIDTH, LANES), lambda i: (0, i)), seq(0)],
        out_specs=[seq(0), seq(0), pl.BlockSpec((CONV_PAD, LANES), lambda i: (0, i)), pl.BlockSpec((1, LANES), lambda i: (0, i))],
        out_shape=[jax.ShapeDtypeStruct((S, C), BF16), jax.ShapeDtypeStruct((S, C), BF16),
                   jax.ShapeDtypeStruct((CONV_PAD, C), F32), jax.ShapeDtypeStruct((1, C), F32)],
        scratch_shapes=[pltpu.VMEM((S + CONV_PAD, LANES), F32), pltpu.VMEM((S + CONV_PAD, LANES), F32),
                        pltpu.VMEM((CONV_PAD, LANES), F32)],
        compiler_params=_params(("parallel",)),
    )(proj, proj, w, dc1)


def _ffn_fwd_call(u, w, b):
    S, C2 = u.shape
    C = C2 // 2
    nt = C // LANES

    def body(ug_ref, uv_ref, wg_ref, wv_ref, bg_ref, bv_ref, f_ref, pg_ref, pv_ref):
        zeros = jnp.zeros((FFN_PAD, LANES), F32)
        pg_ref[0:FFN_PAD, :] = zeros
        pv_ref[0:FFN_PAD, :] = zeros
        pg_ref[FFN_PAD:, :] = ug_ref[...]
        pv_ref[FFN_PAD:, :] = uv_ref[...]

        def step(t0):
            win = pl.ds(t0, TIME_BLOCK + FFN_PAD)
            cg = _conv_block(pg_ref[win, :], wg_ref, FFN_CONV_WIDTH, FFN_PAD) + bg_ref[...]
            cv = _conv_block(pv_ref[win, :], wv_ref, FFN_CONV_WIDTH, FFN_PAD) + bv_ref[...]
            f_ref[pl.ds(t0, TIME_BLOCK), :] = (_gelu(cg) * cv).astype(BF16)

        _time_loop(S, step)

    seq = lambda off: pl.BlockSpec((S, LANES), lambda i: (0, off + i))
    wsp = lambda off: pl.BlockSpec((FFN_CONV_WIDTH, LANES), lambda i: (0, off + i))
    bsp = lambda off: pl.BlockSpec((1, LANES), lambda i: (0, off + i))
    return pl.pallas_call(
        body, name="ffn_conv_geglu", grid=(nt,),
        in_specs=[seq(0), seq(nt), wsp(0), wsp(nt), bsp(0), bsp(nt)],
        out_specs=seq(0), out_shape=jax.ShapeDtypeStruct((S, C), BF16),
        scratch_shapes=[pltpu.VMEM((S + FFN_PAD, LANES), F32)] * 2,
        compiler_params=_params(("parallel",)),
    )(u, u, w, w, b, b)


def _ffn_bwd_call(u, w, b, df):
    S, C2 = u.shape
    C = C2 // 2
    nt = C // LANES

    def body(ug_ref, uv_ref, wg_ref, wv_ref, bg_ref, bv_ref, df_ref,
             dug_ref, duv_ref, dwg_ref, dwv_ref, dbg_ref, dbv_ref,
             pg_ref, pv_ref, dg_ref, dv_ref, dwg_acc, dwv_acc, dbg_acc, dbv_acc):
        zeros = jnp.zeros((FFN_PAD, LANES), F32)
        pg_ref[0:FFN_PAD, :] = zeros
        pv_ref[0:FFN_PAD, :] = zeros
        pg_ref[FFN_PAD:, :] = ug_ref[...]
        pv_ref[FFN_PAD:, :] = uv_ref[...]
        dg_ref[S:, :] = zeros
        dv_ref[S:, :] = zeros
        dwg_acc[...] = jnp.zeros_like(dwg_acc)
        dwv_acc[...] = jnp.zeros_like(dwv_acc)
        dbg_acc[...] = jnp.zeros_like(dbg_acc)
        dbv_acc[...] = jnp.zeros_like(dbv_acc)

        def first(t0):
            win = pl.ds(t0, TIME_BLOCK + FFN_PAD)
            rows = pl.ds(t0, TIME_BLOCK)
            xg = pg_ref[win, :]
            xv = pv_ref[win, :]
            cg = _conv_block(xg, wg_ref, FFN_CONV_WIDTH, FFN_PAD) + bg_ref[...]
            cv = _conv_block(xv, wv_ref, FFN_CONV_WIDTH, FFN_PAD) + bv_ref[...]
            dfb = df_ref[rows, :]
            dcg = dfb * cv * _gelu_grad(cg)
            dcv = dfb * _gelu(cg)
            dg_ref[rows, :] = dcg
            dv_ref[rows, :] = dcv
            _conv_weight_grad(xg, dcg, dwg_acc, FFN_CONV_WIDTH, FFN_PAD)
            _conv_weight_grad(xv, dcv, dwv_acc, FFN_CONV_WIDTH, FFN_PAD)
            dbg_acc[...] += jnp.sum(dcg, axis=0, keepdims=True)
            dbv_acc[...] += jnp.sum(dcv, axis=0, keepdims=True)

        def second(t0):
            win = pl.ds(t0, TIME_BLOCK + FFN_PAD)
            rows = pl.ds(t0, TIME_BLOCK)
            dug_ref[rows, :] = _conv_transpose_block(dg_ref[win, :], wg_ref, FFN_CONV_WIDTH, FFN_PAD).astype(BF16)
            duv_ref[rows, :] = _conv_transpose_block(dv_ref[win, :], wv_ref, FFN_CONV_WIDTH, FFN_PAD).astype(BF16)

        _time_loop(S, first)
        _time_loop(S, second)
        dwg_ref[...] = dwg_acc[...]
        dwv_ref[...] = dwv_acc[...]
        dbg_ref[...] = dbg_acc[...]
        dbv_ref[...] = dbv_acc[...]

    seq = lambda off: pl.BlockSpec((S, LANES), lambda i: (0, off + i))
    wsp = lambda off: pl.BlockSpec((FFN_CONV_WIDTH, LANES), lambda i: (0, off + i))
    bsp = lambda off: pl.BlockSpec((1, LANES), lambda i: (0, off + i))
    return pl.pallas_call(
        body, name="ffn_conv_geglu_bwd", grid=(nt,),
        in_specs=[seq(0), seq(nt), wsp(0), wsp(nt), bsp(0), bsp(nt), seq(0)],
        out_specs=[seq(0), seq(0), pl.BlockSpec((SUBLANES, LANES), lambda i: (0, i)), pl.BlockSpec((SUBLANES, LANES), lambda i: (0, i)),
                   bsp(0), bsp(0)],
        out_shape=[jax.ShapeDtypeStruct((S, C), BF16)] * 2 + [jax.ShapeDtypeStruct((SUBLANES, C), F32)] * 2
        + [jax.ShapeDtypeStruct((1, C), F32)] * 2,
        scratch_shapes=[pltpu.VMEM((S + FFN_PAD, LANES), F32)] * 4 + [pltpu.VMEM((SUBLANES, LANES), F32)] * 2
        + [pltpu.VMEM((1, LANES), F32)] * 2,
        compiler_params=_params(("parallel",)),
    )(u, u, w, w, b, b, df)


def _adamw_call(w, g, m, v, name):
    R, C = w.shape
    tr = _row_tile(R, C)
    c1 = 1.0 / (1.0 - ADAM_B1 ** ADAM_STEP)
    c2 = 1.0 / (1.0 - ADAM_B2 ** ADAM_STEP)

    def body(w_ref, g_ref, m_ref, v_ref, d_ref, mo_ref, vo_ref):
        gv = g_ref[...]
        mn = ADAM_B1 * m_ref[...] + (1.0 - ADAM_B1) * gv
        vn = ADAM_B2 * v_ref[...] + (1.0 - ADAM_B2) * (gv * gv)
        mo_ref[...] = mn
        vo_ref[...] = vn
        d_ref[...] = -ADAM_LR * ((mn * c1) / (jnp.sqrt(vn * c2) + ADAM_EPS) + ADAM_WD * w_ref[...])

    spec = pl.BlockSpec((tr, C), lambda i: (i, 0))
    return pl.pallas_call(
        body, name=name, grid=(R // tr,),
        in_specs=[spec] * 4, out_specs=[spec] * 3,
        out_shape=[jax.ShapeDtypeStruct((R, C), F32)] * 3,
        compiler_params=_params(("parallel",)),
    )(w, g, m, v)


def _position():
    return lax.axis_index("x"), lax.axis_index("y"), lax.axis_index("c")


def _chip_peers(x, y):
    return [(x, 1 - y), (1 - x, y), (1 - x, 1 - y)]


def _half_rows(ref, core, rows):
    h = rows // 2
    start = pl.multiple_of(core * h, 16)
    return ref.at[pl.ds(start, h), :] if len(ref.shape) == 2 else ref.at[:, pl.ds(start, h), :]


def _shard_half(ref, shard, core, rows):
    h = rows // 2
    return ref.at[shard, pl.ds(pl.multiple_of(core * h, 16), h), :]


ANY = pl.BlockSpec(memory_space=pl.ANY)


def _allgather_call(shards, whole):
    n, nw = len(shards), len(whole)
    outs_shape = [jax.ShapeDtypeStruct((N_CHIPS,) + s.shape, s.dtype) for s in shards + whole]

    def body(*refs):
        ins, outs = refs[:n + nw], refs[n + nw:2 * (n + nw)]
        send_sems, recv_sems, pass_send, pass_recv, own_send, own_recv = refs[2 * (n + nw):]
        x, y, c = _position()
        chip = 2 * x + y
        peers = _chip_peers(x, y)
        sent, local = [], []
        for i in range(n + nw):
            cp = pltpu.make_async_remote_copy(src_ref=ins[i], dst_ref=outs[i].at[chip], send_sem=own_send.at[i],
                                              recv_sem=own_recv.at[i], device_id=(x, y, 1 - c), device_id_type=MESH)
            cp.start()
            local.append(cp)
            rows = ins[i].shape[0]
            for k, (px, py) in enumerate(peers):
                if i < n:
                    src, dst = _half_rows(ins[i], c, rows), _shard_half(outs[i], chip, c, rows)
                else:
                    src, dst = ins[i], outs[i].at[chip]
                cp = pltpu.make_async_remote_copy(src_ref=src, dst_ref=dst, send_sem=send_sems.at[i, k],
                                                  recv_sem=recv_sems.at[i, k], device_id=(px, py, c), device_id_type=MESH)
                cp.start()
                sent.append(cp)
        passed = []
        for i in range(n + nw):
            rows = ins[i].shape[0]
            for k, (px, py) in enumerate(peers):
                landed = _shard_half(outs[i], 2 * px + py, c, rows) if i < n else outs[i].at[2 * px + py]
                pltpu.make_async_remote_copy(src_ref=landed, dst_ref=landed, send_sem=send_sems.at[i, k],
                                             recv_sem=recv_sems.at[i, k], device_id=(px, py, c), device_id_type=MESH).wait_recv()
                if i < n:
                    cp = pltpu.make_async_remote_copy(src_ref=landed, dst_ref=landed, send_sem=pass_send.at[i, k],
                                                      recv_sem=pass_recv.at[i, k], device_id=(x, y, 1 - c), device_id_type=MESH)
                    cp.start()
                    passed.append(cp)
        for cp in sent:
            cp.wait_send()
        for cp in passed:
            cp.wait()
        for cp in local:
            cp.wait()

    return pl.pallas_call(
        body, name="weight_allgather",
        in_specs=[ANY] * (n + nw), out_specs=[ANY] * (n + nw), out_shape=outs_shape,
        scratch_shapes=[pltpu.SemaphoreType.DMA((n + nw, 3)), pltpu.SemaphoreType.DMA((n + nw, 3)),
                        pltpu.SemaphoreType.DMA((n, 3)), pltpu.SemaphoreType.DMA((n, 3)),
                        pltpu.SemaphoreType.DMA((n + nw,)), pltpu.SemaphoreType.DMA((n + nw,))],
    )(*shards, *whole)


HBM_SPEC = pl.BlockSpec(memory_space=pltpu.HBM)
SEM_SPEC = pl.BlockSpec(memory_space=pltpu.SEMAPHORE)
DATAFLOW = pltpu.SideEffectType.DATAFLOW_SIDE_EFFECTING


def _in_hbm(a):
    return pltpu.with_memory_space_constraint(a, pltpu.HBM)


def _split_start(name, srcs, lands, n_sems, copies, after):
    n, m = len(srcs), len(lands)

    def body(*refs):
        src_refs, land_refs = refs[:n], refs[n:n + m]
        send_sem, recv_sem = refs[n + m + 1], refs[n + m + 2]
        token = refs[-1]
        for src, dst, dev, idx in copies(src_refs, land_refs):
            pltpu.make_async_remote_copy(src_ref=src, dst_ref=dst, send_sem=send_sem.at[idx], recv_sem=recv_sem.at[idx],
                                         device_id=dev, device_id_type=MESH).start()
        token[...] = jnp.zeros_like(token)

    outs = pl.pallas_call(
        body, name=name,
        in_specs=[HBM_SPEC] * (n + m) + [ANY],
        out_specs=[SEM_SPEC, SEM_SPEC] + [HBM_SPEC] * (n + m) + [pl.BlockSpec(memory_space=pltpu.VMEM)],
        out_shape=[pltpu.SemaphoreType.DMA((n_sems,)), pltpu.SemaphoreType.DMA((n_sems,))]
        + [pltpu.HBM(a.shape, a.dtype) for a in list(srcs) + list(lands)] + [jax.ShapeDtypeStruct((SUBLANES, LANES), F32)],
        input_output_aliases={i: 2 + i for i in range(n + m)},
        compiler_params=pltpu.CompilerParams(has_side_effects=DATAFLOW),
    )(*[_in_hbm(a) for a in list(srcs) + list(lands)], after)
    return dict(send=outs[0], recv=outs[1], srcs=list(outs[2:2 + n]), lands=list(outs[2 + n:2 + n + m]),
                tile=outs[-1], token=outs[-1][0, 0])


def _split_wait(name, started, copies, after):
    n, m = len(started["srcs"]), len(started["lands"])

    def body(*refs):
        src_refs, land_refs = refs[:n], refs[n:n + m]
        send_sem, recv_sem = refs[n + m], refs[n + m + 1]
        for src, dst, dev, idx in copies(src_refs, land_refs):
            cp = pltpu.make_async_remote_copy(src_ref=src, dst_ref=dst, send_sem=send_sem.at[idx], recv_sem=recv_sem.at[idx],
                                              device_id=dev, device_id_type=MESH)
            cp.wait_send()
            cp.wait_recv()

    arrays = started["srcs"] + started["lands"]
    outs = pl.pallas_call(
        body, name=name,
        in_specs=[HBM_SPEC] * (n + m) + [SEM_SPEC, SEM_SPEC, ANY],
        out_specs=[HBM_SPEC] * (n + m),
        out_shape=[pltpu.HBM(a.shape, a.dtype) for a in arrays],
        input_output_aliases={i: i for i in range(n + m)},
        compiler_params=pltpu.CompilerParams(has_side_effects=DATAFLOW),
    )(*arrays, started["send"], started["recv"], after)
    return list(outs)


def _gather_copies(srcs, lands):
    x, y, c = _position()
    chip = 2 * x + y
    targets = [(px, py, c) for px, py in _chip_peers(x, y)] + [(x, y, 1 - c)]
    return [(s, l.at[chip], dev, len(targets) * i + k) for i, (s, l) in enumerate(zip(srcs, lands)) for k, dev in enumerate(targets)]


def _sibling_copies(srcs, lands):
    x, y, c = _position()
    return [(_half_rows(srcs[0], 1 - c, srcs[0].shape[1]), lands[0], (x, y, 1 - c), 0)]


def _exchange_copies(srcs, lands):
    x, y, c = _position()
    return [(srcs[0].at[2 * px + py], lands[0].at[k], (px, py, c), k) for k, (px, py) in enumerate(_chip_peers(x, y))]


def _pair_sum_call(grad, recv, core, name):
    _, h, B = recv.shape
    tr = _row_tile(h, B)

    def body(core_ref, g_ref, r_ref, o_ref, ob_ref):
        s = g_ref[...] + r_ref[...]
        o_ref[...] = s
        ob_ref[...] = s.astype(BF16)

    g_spec = pl.BlockSpec((None, tr, B), lambda q, i, core_ref: (q, core_ref[0] * (h // tr) + i, 0))
    spec = pl.BlockSpec((None, tr, B), lambda q, i, core_ref: (q, i, 0))
    return pl.pallas_call(
        body, name=name,
        grid_spec=pltpu.PrefetchScalarGridSpec(num_scalar_prefetch=1, grid=(N_CHIPS, h // tr), in_specs=[g_spec, spec],
                                               out_specs=[spec, spec]),
        out_shape=[jax.ShapeDtypeStruct(recv.shape, F32), jax.ShapeDtypeStruct(recv.shape, BF16)],
        compiler_params=_params(("parallel", "parallel")),
    )(core, grad, recv)


def _chip_sum_call(partial, recv, chip_core, name):
    _, h, B = recv.shape
    tr = _row_tile(h, B)

    def body(cc_ref, p_ref, r_ref, o_ref):
        o_ref[...] = ((p_ref[...] + r_ref[0].astype(F32)) + r_ref[1].astype(F32)) + r_ref[2].astype(F32)

    return pl.pallas_call(
        body, name=name,
        grid_spec=pltpu.PrefetchScalarGridSpec(
            num_scalar_prefetch=1, grid=(h // tr,),
            in_specs=[pl.BlockSpec((None, tr, B), lambda i, cc_ref: (cc_ref[0], i, 0)),
                      pl.BlockSpec((3, tr, B), lambda i, cc_ref: (0, i, 0))],
            out_specs=pl.BlockSpec((tr, B), lambda i, cc_ref: (cc_ref[1] * (h // tr) + i, 0))),
        out_shape=jax.ShapeDtypeStruct((2 * h, B), F32),
        compiler_params=_params(("parallel",)),
    )(chip_core, partial, recv)


def _sibling_assemble_call(shards, name="grad_sibling_assemble"):
    n = len(shards)

    def body(*refs):
        ins, outs = refs[:n], refs[n:2 * n]
        send_sems, recv_sems = refs[2 * n:]
        x, y, c = _position()
        copies = []
        for i in range(n):
            rows = shards[i].shape[0]
            cp = pltpu.make_async_remote_copy(src_ref=_half_rows(ins[i], c, rows), dst_ref=_half_rows(outs[i], c, rows),
                                              send_sem=send_sems.at[i], recv_sem=recv_sems.at[i],
                                              device_id=(x, y, 1 - c), device_id_type=MESH)
            cp.start()
            copies.append(cp)
        for cp in copies:
            cp.wait()

    return pl.pallas_call(
        body, name=name,
        in_specs=[ANY] * n, out_specs=[ANY] * n,
        out_shape=[jax.ShapeDtypeStruct(s.shape, F32) for s in shards],
        input_output_aliases={i: i for i in range(n)},
        scratch_shapes=[pltpu.SemaphoreType.DMA((n,)), pltpu.SemaphoreType.DMA((n,))],
    )(*shards)


def _small_allreduce_call(packed):
    rows = packed.shape[0]

    def body(x_ref, o_ref, buf_ref, send_sems, recv_sems):
        x, y, c = _position()
        me = 4 * x + 2 * y + c
        buf_ref[me] = x_ref[...]
        copies = []
        for k in range(1, 8):
            peer = (1 - x if k & 4 else x, 1 - y if k & 2 else y, 1 - c if k & 1 else c)
            cp = pltpu.make_async_remote_copy(src_ref=buf_ref.at[me], dst_ref=buf_ref.at[me], send_sem=send_sems.at[k - 1],
                                              recv_sem=recv_sems.at[k - 1], device_id=peer, device_id_type=MESH)
            cp.start()
            copies.append(cp)
        for cp in copies:
            cp.wait()
        acc = buf_ref[0]
        for d in range(1, 8):
            acc = acc + buf_ref[d]
        o_ref[...] = acc

    return pl.pallas_call(
        body, name="small_grad_allreduce",
        in_specs=[pl.BlockSpec(memory_space=pltpu.VMEM)], out_specs=pl.BlockSpec(memory_space=pltpu.VMEM),
        out_shape=jax.ShapeDtypeStruct((rows, LANES), F32),
        scratch_shapes=[pltpu.VMEM((8, rows, LANES), F32), pltpu.SemaphoreType.DMA((7,)), pltpu.SemaphoreType.DMA((7,))],
    )(packed)


def _pack(arrays):
    flat = jnp.concatenate([a.reshape(-1).astype(F32) for a in arrays])
    rows = -(-flat.shape[0] // LANES)
    rows = -(-rows // SUBLANES) * SUBLANES
    flat = jnp.pad(flat, (0, rows * LANES - flat.shape[0]))
    return flat.reshape(rows, LANES)


def _unpack(packed, shapes):
    flat = packed.reshape(-1)
    out, off = [], 0
    for shp in shapes:
        size = int(np.prod(shp))
        out.append(flat[off:off + size].reshape(shp))
        off += size
    return out


def _local_step(xs, target, P, late_weights, on_grad):
    S, D = xs.shape
    qkv_width = 3 * N_HEADS * HEAD_DIM
    glu_col0, gate_col0 = qkv_width, qkv_width + 2 * D
    shard_major = lambda g: g.reshape(N_CHIPS, g.shape[0] // N_CHIPS, g.shape[1])

    h1 = _rms_fwd_call(xs, P["norm_mix_pre"])
    proj = _matmul(h1, P["w_in"], "nn", "proj_in")
    buckets = _bucket_tables()
    bias = _bias_table_call(P["rel_bias"], buckets)
    parts = []
    for g in range(N_GROUPS):
        parts += _attn_fwd_call(proj, bias, g)
    a, a_bf, lse = _attn_merge_call(parts)
    P = dict(P, **late_weights("mix", a_bf))
    y_a = _matmul(a_bf, P["w_attn_out"], "nn", "attn_out")
    c1 = _conv_fwd_call(proj, glu_col0, P["conv_dw_w"], P["conv_dw_b"])
    cact = _ln_silu_call(c1, P["conv_ln_g"], P["conv_ln_b"])
    y_c = _matmul(cact, P["conv_pw_w"], "nn", "conv_pw")
    mixed = _mix_call(proj, gate_col0, P["b_gate"], y_a, y_c)
    out = _matmul(mixed, P["w_out"], "nn", "mix_out")
    x1, h2 = _res1_call(xs, out, P["norm_mix_post"], P["norm_ffn_pre"])
    P = dict(P, **late_weights("ffn", h2))
    u = _matmul(h2, P["w_up"], "nn", "ffn_up")
    f = _ffn_fwd_call(u, P["ffn_conv_w"], P["ffn_conv_b"])
    yff = _matmul(f, P["w_down"], "nn", "ffn_down")
    loss_tile, dx2 = _loss_call(yff, x1, P["norm_ffn_post"], target)

    G = {}
    dyff, G["norm_ffn_post"] = _rms_bwd_call(yff, P["norm_ffn_post"], dx2, "rms_ffn_post_bwd")
    zero = on_grad("w_down", shard_major(_matmul(f, dyff, "tn", "ffn_down_dw")))
    df = _matmul(dyff, P["w_down"], "nt", "ffn_down_dx")
    dug, duv, dwg, dwv, dbg, dbv = _ffn_bwd_call(u, P["ffn_conv_w"], P["ffn_conv_b"] + zero, df)
    G["ffn_conv_w"] = jnp.concatenate([dwg[:FFN_CONV_WIDTH], dwv[:FFN_CONV_WIDTH]], axis=1)
    G["ffn_conv_b"] = jnp.concatenate([dbg, dbv], axis=1)
    du = jnp.concatenate([dug, duv], axis=1)
    zero = on_grad("w_up", _matmul(h2, du, "tn", "ffn_up_dw", out_shards=True))
    dh2 = _matmul(du, P["w_up"], "nt", "ffn_up_dx")
    dx1, dout, G["norm_ffn_pre"], G["norm_mix_post"] = _mid_bwd_call(x1, P["norm_ffn_pre"] + zero, dh2, dx2, out, P["norm_mix_post"])
    zero = on_grad("w_out", shard_major(_matmul(mixed, dout, "tn", "mix_out_dw")))
    dmixed = _matmul(dout, P["w_out"], "nt", "mix_out_dx")
    dya, dyc, dga, dgc, dba, dbc = _mix_bwd_call(dmixed, proj, gate_col0, P["b_gate"] + zero, y_a, y_c)
    G["b_gate"] = jnp.concatenate([dba, dbc], axis=1)
    zero = on_grad("w_attn_out", _matmul(a_bf, dya, "tn", "attn_out_dw", out_shards=True))
    zero = zero + on_grad("conv_pw_w", shard_major(_matmul(cact, dyc, "tn", "conv_pw_dw")))
    da = _matmul(dya, P["w_attn_out"], "nt", "attn_out_dx")
    dcact = _matmul(dyc, P["conv_pw_w"], "nt", "conv_pw_dx")
    dc1, G["conv_ln_g"], G["conv_ln_b"] = _ln_silu_bwd_call(c1, P["conv_ln_g"] + zero, P["conv_ln_b"], dcact)
    dval, dgate, dw_dw, G["conv_dw_b"] = _conv_bwd_call(proj, glu_col0, P["conv_dw_w"], dc1)
    G["conv_dw_w"] = dw_dw[:CONV_WIDTH]
    delta = _attn_delta_call(a, da)
    dqs, dks, dvs, dbs = [], [], [], []
    for g in range(N_GROUPS):
        dq, dk, dv, db = _attn_bwd_call(proj, bias, da, lse, delta, g)
        dqs.append(dq)
        dks.append(dk)
        dvs.append(dv)
        dbs.append(db)
    G["rel_bias"] = _bias_grad_call(jnp.concatenate(dbs, axis=0), buckets)
    dproj = _dproj_call(dqs + dks + dvs, [dval, dgate, dga, dgc])
    zero = on_grad("w_in", _matmul(h1, dproj, "tn", "proj_in_dw", out_shards=True, tm=512))
    dh1 = _matmul(dproj, P["w_in"], "nt", "proj_in_dx")
    grad_x, G["norm_mix_pre"] = _in_bwd_call(xs, P["norm_mix_pre"] + zero, dh1, dx1)
    return loss_tile, grad_x, G


def kernel(x, w_in, b_gate, rel_bias, w_attn_out, conv_dw_w, conv_dw_b, conv_ln_g, conv_ln_b, conv_pw_w, w_out, norm_mix_pre, norm_mix_post, norm_ffn_pre, norm_ffn_post, w_up, ffn_conv_w, ffn_conv_b, w_down, loss_target, m_w_in, m_b_gate, m_rel_bias, m_w_attn_out, m_conv_dw_w, m_conv_dw_b, m_conv_ln_g, m_conv_ln_b, m_conv_pw_w, m_w_out, m_norm_mix_pre, m_norm_mix_post, m_norm_ffn_pre, m_norm_ffn_post, m_w_up, m_ffn_conv_w, m_ffn_conv_b, m_w_down, v_w_in, v_b_gate, v_rel_bias, v_w_attn_out, v_conv_dw_w, v_conv_dw_b, v_conv_ln_g, v_conv_ln_b, v_conv_pw_w, v_w_out, v_norm_mix_pre, v_norm_mix_post, v_norm_ffn_pre, v_norm_ffn_post, v_w_up, v_ffn_conv_w, v_ffn_conv_b, v_w_down):
    weights = dict(w_in=w_in, b_gate=b_gate, rel_bias=rel_bias, w_attn_out=w_attn_out, conv_dw_w=conv_dw_w, conv_dw_b=conv_dw_b,
                   conv_ln_g=conv_ln_g, conv_ln_b=conv_ln_b, conv_pw_w=conv_pw_w, w_out=w_out, norm_mix_pre=norm_mix_pre,
                   norm_mix_post=norm_mix_post, norm_ffn_pre=norm_ffn_pre, norm_ffn_post=norm_ffn_post, w_up=w_up,
                   ffn_conv_w=ffn_conv_w, ffn_conv_b=ffn_conv_b, w_down=w_down)
    m_in = dict(w_in=m_w_in, b_gate=m_b_gate, rel_bias=m_rel_bias, w_attn_out=m_w_attn_out, conv_dw_w=m_conv_dw_w,
                conv_dw_b=m_conv_dw_b, conv_ln_g=m_conv_ln_g, conv_ln_b=m_conv_ln_b, conv_pw_w=m_conv_pw_w, w_out=m_w_out,
                norm_mix_pre=m_norm_mix_pre, norm_mix_post=m_norm_mix_post, norm_ffn_pre=m_norm_ffn_pre,
                norm_ffn_post=m_norm_ffn_post, w_up=m_w_up, ffn_conv_w=m_ffn_conv_w, ffn_conv_b=m_ffn_conv_b, w_down=m_w_down)
    v_in = dict(w_in=v_w_in, b_gate=v_b_gate, rel_bias=v_rel_bias, w_attn_out=v_w_attn_out, conv_dw_w=v_conv_dw_w,
                conv_dw_b=v_conv_dw_b, conv_ln_g=v_conv_ln_g, conv_ln_b=v_conv_ln_b, conv_pw_w=v_conv_pw_w, w_out=v_w_out,
                norm_mix_pre=v_norm_mix_pre, norm_mix_post=v_norm_mix_post, norm_ffn_pre=v_norm_ffn_pre,
                norm_ffn_post=v_norm_ffn_post, w_up=v_w_up, ffn_conv_w=v_ffn_conv_w, ffn_conv_b=v_ffn_conv_b, w_down=v_w_down)
    names = list(weights)
    xi, yi, ci = _position()
    chip = 2 * xi + yi
    core_arr = jnp.reshape(ci, (1,)).astype(jnp.int32)

    xs = x[0]
    target = loss_target[0]
    S, D = xs.shape

    big = ["w_in", "w_attn_out", "conv_pw_w", "w_out", "w_up", "w_down"]
    row_sharded = ("conv_pw_w", "w_out", "w_down")
    bf16_shard = {k: weights[k][0].astype(BF16) for k in big}
    natural = lambda k, g: g.reshape(-1, g.shape[2]) if k in row_sharded else g
    w_in_full, dw4, fc4 = _allgather_call([bf16_shard["w_in"]], [conv_dw_w[0], ffn_conv_w[0]])
    late_sets = dict(mix=["w_attn_out", "conv_pw_w", "w_out"], ffn=["w_up", "w_down"])
    started, after = {}, w_in_full
    for tag, keys in late_sets.items():
        srcs = [bf16_shard[k] for k in keys]
        lands = [lax.empty((N_CHIPS,) + s.shape, BF16) for s in srcs]
        started[tag] = _split_start(f"gather_{tag}_start", srcs, lands, 4 * len(keys), _gather_copies, after)
        after = started[tag]["tile"]
    launched = started["mix"]["token"] + started["ffn"]["token"]

    def late_weights(tag, after):
        landed = _split_wait(f"gather_{tag}_wait", started[tag], _gather_copies, after)[len(late_sets[tag]):]
        return {k: natural(k, g) for k, g in zip(late_sets[tag], landed)}

    chip_core = jnp.stack([chip, ci]).astype(jnp.int32)
    exchanging, pending = {}, {}

    def pair_up(after):
        token = jnp.float32(0.0)
        for k in list(exchanging):
            g3, r1 = _split_wait(f"sibling_exchange_wait_{k}", exchanging.pop(k), _sibling_copies, after)
            s32, s16 = _pair_sum_call(g3, r1, core_arr, f"pair_sum_{k}")
            land = lax.empty((3,) + s16.shape[1:], BF16)
            pending[k] = (s32, _split_start(f"chip_exchange_start_{k}", [s16], [land], 3, _exchange_copies, s32))
            token = token + pending[k][1]["token"]
        return token

    def on_grad(k, g3):
        token = pair_up(g3[0, :SUBLANES, :LANES])
        land = lax.empty((N_CHIPS, g3.shape[1] // 2, g3.shape[2]), F32)
        exchanging[k] = _split_start(f"sibling_exchange_start_{k}", [g3], [land], 1, _sibling_copies, core_arr)
        return token + exchanging[k]["token"]

    def finish(keys, after, tag):
        halves = []
        for k in keys:
            s32, st = pending[k]
            recv2 = _split_wait(f"chip_exchange_wait_{k}", st, _exchange_copies, after)[1]
            halves.append(_chip_sum_call(s32, recv2, chip_core, f"chip_sum_{k}"))
        return dict(zip(keys, _sibling_assemble_call(halves, f"grad_sibling_assemble_{tag}")))

    P = dict(w_in=w_in_full, conv_dw_w=jnp.concatenate(list(dw4), axis=1), ffn_conv_w=jnp.concatenate(list(fc4), axis=1),
             b_gate=b_gate, rel_bias=rel_bias, conv_dw_b=conv_dw_b, conv_ln_g=conv_ln_g, conv_ln_b=conv_ln_b,
             norm_mix_pre=norm_mix_pre + launched, norm_mix_post=norm_mix_post, norm_ffn_pre=norm_ffn_pre,
             norm_ffn_post=norm_ffn_post, ffn_conv_b=ffn_conv_b)
    loss_tile, grad_x, G = _local_step(xs, target, P, late_weights, on_grad)

    small = [k for k in names if k not in big]
    packed = _pack([loss_tile[:1]] + [G[k] for k in small])
    summed_block = _small_allreduce_call(packed)
    summed_block = summed_block + pair_up(summed_block[:SUBLANES])
    loss_row, *summed = _unpack(summed_block, [(1, LANES)] + [G[k].shape for k in small])
    loss = loss_row[0, 0]
    reduced = {}
    for k, gsum in zip(small, summed):
        if k in ("conv_dw_w", "ffn_conv_w"):
            cols = weights[k].shape[2]
            reduced[k] = lax.dynamic_slice_in_dim(gsum, chip * cols, cols, axis=1)
        else:
            reduced[k] = gsum

    grads, deltas, new_m, new_v = {}, {}, {}, {}

    def update(keys):
        for k in keys:
            d, mn, vn = _adamw_call(weights[k][0], reduced[k], m_in[k][0], v_in[k][0], f"adamw_{k}")
            grads[k], deltas[k], new_m[k], new_v[k] = reduced[k][None], d[None], mn[None], vn[None]

    others = [k for k in big if k != "w_in"]
    reduced.update(finish(others, summed_block, "others"))
    update(others)
    reduced.update(finish(["w_in"], deltas["w_up"], "w_in"))
    update(["w_in"])
    flat2 = lambda t: t.reshape(-1, t.shape[-1]) if t.ndim == 3 else t
    pw = _pack([flat2(weights[k]) for k in small])
    pg = _pack([reduced[k] for k in small])
    pm = _pack([flat2(m_in[k]) for k in small])
    pv = _pack([flat2(v_in[k]) for k in small])
    pd, pmn, pvn = _adamw_call(pw, pg, pm, pv, "adamw_small")
    shapes = [weights[k].shape for k in small]
    for k, gk, dk_, mk, vk in zip(small, [reduced[k] for k in small], _unpack(pd, shapes), _unpack(pmn, shapes), _unpack(pvn, shapes)):
        grads[k], deltas[k], new_m[k], new_v[k] = gk.reshape(weights[k].shape), dk_, mk, vk

    return (loss, grad_x[None], *[grads[k] for k in names], *[deltas[k] for k in names],
            *[new_m[k] for k in names], *[new_v[k] for k in names])
```

```python
import functools
import math

import jax
import jax.numpy as jnp
import numpy as np
from jax import lax
from jax.experimental import pallas as pl
from jax.experimental.pallas import tpu as pltpu

F32 = jnp.float32
BF16 = jnp.bfloat16
MESH = pl.DeviceIdType.MESH

HEAD_DIM = 128
HEADS_PER_GROUP = 4
DILATED_PATTERNS = ((128, 1), (512, 4), (2048, 16))
N_GROUPS = 3
N_HEADS = N_GROUPS * HEADS_PER_GROUP
SPAN = 128
GROUP_WIDTH = HEADS_PER_GROUP * HEAD_DIM
CONV_WIDTH = 31
FFN_CONV_WIDTH = 3
N_BUCKETS = 32
MAX_DISTANCE = 2048
RMS_EPS = 1e-6
LN_EPS = 1e-5
NEG_INF = -1e30
ADAM_LR = 0.001
ADAM_B1 = 0.9
ADAM_B2 = 0.999
ADAM_EPS = 1e-08
ADAM_WD = 0.01
ADAM_STEP = 10

LANES = 128
SUBLANES = 8
ROW_TILE = 256
TIME_BLOCK = 128
CONV_PAD = 32
FFN_PAD = 8
VMEM_LIMIT = 56 << 20


def _params(sem=None, vmem=None):
    kw = {}
    if sem is not None:
        kw["dimension_semantics"] = sem
    if vmem is not None:
        kw["vmem_limit_bytes"] = vmem
    return pltpu.CompilerParams(**kw)


def _pick(n, cands):
    for c in cands:
        if n % c == 0:
            return c
    return n


ELEMENTWISE_TILE_BYTES = 3 << 19


def _row_tile(rows, cols):
    for align in (16, SUBLANES):
        fits = [t for t in range(align, rows + 1, align) if rows % t == 0 and t * cols * 4 <= ELEMENTWISE_TILE_BYTES]
        if fits:
            return max(fits)
    return SUBLANES


N_CHIPS = 4
M_TILES = (1024, 1408, 512, 256, 128)
N_TILES = (512, 1408, 256, 128)
K_TILES = (2176, 2048, 1408, 1024, 512, 256, 128)


def _matmul(a, b, mode, name, out_shards=False, tm=None):
    assert a.dtype == BF16 and b.dtype == BF16, (name, a.dtype, b.dtype)
    b3 = b.ndim == 3
    tn = tk = None
    if mode == "nn":
        M, K = a.shape
        N = b.shape[-1] * (N_CHIPS if b3 else 1)
        tn = b.shape[-1] if b3 else None
    elif mode == "nt":
        M, K = a.shape
        N = b.shape[-2]
        tk = b.shape[-1] if b3 else None
    else:
        K, M = a.shape
        N = b.shape[1]
        tn = N // N_CHIPS if out_shards else None
    tm = tm or _pick(M, M_TILES)
    tn = tn or _pick(N, N_TILES)
    tk = tk or _pick(K, K_TILES)
    nk = K // tk
    dn = {"nn": (((1,), (0,)), ((), ())), "nt": (((1,), (1,)), ((), ())), "tn": (((0,), (0,)), ((), ()))}[mode]

    def body(a_ref, b_ref, o_ref):
        if nk == 1:
            o_ref[...] = lax.dot_general(a_ref[...], b_ref[...], dn, preferred_element_type=F32)
        else:
            @pl.when(pl.program_id(2) == 0)
            def _():
                o_ref[...] = jnp.zeros_like(o_ref)

            o_ref[...] += lax.dot_general(a_ref[...], b_ref[...], dn, preferred_element_type=F32)

    if mode == "tn":
        a_spec = pl.BlockSpec((tk, tm), lambda i, j, k: (k, i))
    else:
        a_spec = pl.BlockSpec((tm, tk), lambda i, j, k: (i, k))
    if mode == "nn":
        b_spec = pl.BlockSpec((None, tk, tn), lambda i, j, k: (j, k, 0)) if b3 else pl.BlockSpec((tk, tn), lambda i, j, k: (k, j))
    elif mode == "nt":
        b_spec = pl.BlockSpec((None, tn, tk), lambda i, j, k: (k, j, 0)) if b3 else pl.BlockSpec((tn, tk), lambda i, j, k: (j, k))
    else:
        b_spec = pl.BlockSpec((tk, tn), lambda i, j, k: (k, j))
    if out_shards:
        out_spec = pl.BlockSpec((None, tm, tn), lambda i, j, k: (j, i, 0))
        out_shape = jax.ShapeDtypeStruct((N_CHIPS, M, tn), F32)
    else:
        out_spec = pl.BlockSpec((tm, tn), lambda i, j, k: (i, j))
        out_shape = jax.ShapeDtypeStruct((M, N), F32)
    return pl.pallas_call(
        body, name=name, grid=(M // tm, N // tn, nk),
        in_specs=[a_spec, b_spec], out_specs=out_spec, out_shape=out_shape,
        compiler_params=_params(("parallel", "parallel", "arbitrary"), VMEM_LIMIT),
    )(a, b)


def _rms(x, g):
    r = lax.rsqrt(jnp.mean(x * x, axis=-1, keepdims=True) + RMS_EPS)
    return x * r * g


def _rms_bwd(x, g, dy):
    r = lax.rsqrt(jnp.mean(x * x, axis=-1, keepdims=True) + RMS_EPS)
    n = x * r
    dn = dy * g
    dx = r * (dn - n * jnp.mean(dn * n, axis=-1, keepdims=True))
    return dx, jnp.sum(dy * n, axis=0, keepdims=True)


def _sigmoid(x):
    return 1.0 / (1.0 + jnp.exp(-x))


_GELU_C = math.sqrt(2.0 / math.pi)


def _gelu(x):
    return 0.5 * x * (1.0 + jnp.tanh(_GELU_C * (x + 0.044715 * x * x * x)))


def _gelu_grad(x):
    t = jnp.tanh(_GELU_C * (x + 0.044715 * x * x * x))
    return 0.5 * (1.0 + t) + 0.5 * x * (1.0 - t * t) * _GELU_C * (1.0 + 3.0 * 0.044715 * x * x)


def _row_spec(width, col_block=0):
    return pl.BlockSpec((ROW_TILE, width), lambda i: (i, col_block))


def _vec_spec(width, col_block=0):
    return pl.BlockSpec((1, width), lambda i: (0, col_block))


def _accumulate(ref, part):
    @pl.when(pl.program_id(0) == 0)
    def _():
        ref[...] = part

    @pl.when(pl.program_id(0) > 0)
    def _():
        ref[...] += part


def _rms_fwd_call(x, g):
    S, D = x.shape

    def body(x_ref, g_ref, h_ref):
        h_ref[...] = _rms(x_ref[...], g_ref[...]).astype(BF16)

    return pl.pallas_call(
        body, name="rms_mix_pre", grid=(S // ROW_TILE,),
        in_specs=[_row_spec(D), _vec_spec(D)], out_specs=_row_spec(D),
        out_shape=jax.ShapeDtypeStruct((S, D), BF16),
        compiler_params=_params(("parallel",)),
    )(x, g)


def _ln_silu_call(c1, g, b):
    S, C = c1.shape

    def body(c_ref, g_ref, b_ref, o_ref):
        xv = c_ref[...]
        mu = jnp.mean(xv, axis=-1, keepdims=True)
        xc = xv - mu
        var = jnp.mean(xc * xc, axis=-1, keepdims=True)
        z = xc * lax.rsqrt(var + LN_EPS) * g_ref[...] + b_ref[...]
        o_ref[...] = (z * _sigmoid(z)).astype(BF16)

    return pl.pallas_call(
        body, name="conv_ln_silu", grid=(S // ROW_TILE,),
        in_specs=[_row_spec(C), _vec_spec(C), _vec_spec(C)], out_specs=_row_spec(C),
        out_shape=jax.ShapeDtypeStruct((S, C), BF16),
        compiler_params=_params(("parallel",)),
    )(c1, g, b)


def _ln_silu_bwd_call(c1, g, b, dc):
    S, C = c1.shape

    def body(c_ref, g_ref, b_ref, dc_ref, dx_ref, dg_ref, db_ref):
        xv = c_ref[...]
        mu = jnp.mean(xv, axis=-1, keepdims=True)
        xc = xv - mu
        rs = lax.rsqrt(jnp.mean(xc * xc, axis=-1, keepdims=True) + LN_EPS)
        xh = xc * rs
        z = xh * g_ref[...] + b_ref[...]
        sg = _sigmoid(z)
        dz = dc_ref[...] * (sg * (1.0 + z * (1.0 - sg)))
        dxh = dz * g_ref[...]
        dx_ref[...] = rs * (dxh - jnp.mean(dxh, axis=-1, keepdims=True) - xh * jnp.mean(dxh * xh, axis=-1, keepdims=True))
        _accumulate(dg_ref, jnp.sum(dz * xh, axis=0, keepdims=True))
        _accumulate(db_ref, jnp.sum(dz, axis=0, keepdims=True))

    return pl.pallas_call(
        body, name="conv_ln_silu_bwd", grid=(S // ROW_TILE,),
        in_specs=[_row_spec(C), _vec_spec(C), _vec_spec(C), _row_spec(C)],
        out_specs=[_row_spec(C), _vec_spec(C), _vec_spec(C)],
        out_shape=[jax.ShapeDtypeStruct((S, C), F32), jax.ShapeDtypeStruct((1, C), F32), jax.ShapeDtypeStruct((1, C), F32)],
        compiler_params=_params(("arbitrary",)),
    )(c1, g, b, dc)


def _mix_call(proj, gate_col0, b_gate, y_a, y_c):
    S, D = y_a.shape
    w = 512
    nc = D // w
    ga0, gc0 = gate_col0 // w, (gate_col0 + D) // w

    def body(ga_ref, gc_ref, ba_ref, bc_ref, ya_ref, yc_ref, o_ref):
        o_ref[...] = (_sigmoid(ga_ref[...] + ba_ref[...]) * ya_ref[...]
                      + _sigmoid(gc_ref[...] + bc_ref[...]) * yc_ref[...]).astype(BF16)

    tile = lambda off: pl.BlockSpec((ROW_TILE, w), lambda i, j: (i, off + j))
    vec = lambda off: pl.BlockSpec((1, w), lambda i, j: (0, off + j))
    return pl.pallas_call(
        body, name="gate_mix", grid=(S // ROW_TILE, nc),
        in_specs=[tile(ga0), tile(gc0), vec(0), vec(nc), tile(0), tile(0)],
        out_specs=tile(0), out_shape=jax.ShapeDtypeStruct((S, D), BF16),
        compiler_params=_params(("parallel", "parallel")),
    )(proj, proj, b_gate, b_gate, y_a, y_c)


def _mix_bwd_call(dmixed, proj, gate_col0, b_gate, y_a, y_c):
    S, D = y_a.shape
    w = 512
    nc = D // w
    ga0, gc0 = gate_col0 // w, (gate_col0 + D) // w

    def body(dm_ref, ga_ref, gc_ref, ba_ref, bc_ref, ya_ref, yc_ref, dya_ref, dyc_ref, dga_ref, dgc_ref, dba_ref, dbc_ref):
        dm = dm_ref[...]
        sa = _sigmoid(ga_ref[...] + ba_ref[...])
        sc = _sigmoid(gc_ref[...] + bc_ref[...])
        dya_ref[...] = (dm * sa).astype(BF16)
        dyc_ref[...] = (dm * sc).astype(BF16)
        dga = dm * ya_ref[...] * sa * (1.0 - sa)
        dgc = dm * yc_ref[...] * sc * (1.0 - sc)
        dga_ref[...] = dga.astype(BF16)
        dgc_ref[...] = dgc.astype(BF16)
        pa = jnp.sum(dga, axis=0, keepdims=True)
        pc = jnp.sum(dgc, axis=0, keepdims=True)

        @pl.when(pl.program_id(1) == 0)
        def _():
            dba_ref[...] = pa
            dbc_ref[...] = pc

        @pl.when(pl.program_id(1) > 0)
        def _():
            dba_ref[...] += pa
            dbc_ref[...] += pc

    tile = lambda off: pl.BlockSpec((ROW_TILE, w), lambda j, i: (i, off + j))
    vec = lambda off: pl.BlockSpec((1, w), lambda j, i: (0, off + j))
    return pl.pallas_call(
        body, name="gate_mix_bwd", grid=(nc, S // ROW_TILE),
        in_specs=[tile(0), tile(ga0), tile(gc0), vec(0), vec(nc), tile(0), tile(0)],
        out_specs=[tile(0), tile(0), tile(0), tile(0), vec(0), vec(0)],
        out_shape=[jax.ShapeDtypeStruct((S, D), BF16)] * 4 + [
                   jax.ShapeDtypeStruct((1, D), F32), jax.ShapeDtypeStruct((1, D), F32)],
        compiler_params=_params(("parallel", "arbitrary")),
    )(dmixed, proj, proj, b_gate, b_gate, y_a, y_c)


def _res1_call(x, out, g_post, g_pre):
    S, D = x.shape

    def body(x_ref, o_ref, gp_ref, gq_ref, x1_ref, h2_ref):
        x1 = x_ref[...] + _rms(o_ref[...], gp_ref[...])
        x1_ref[...] = x1
        h2_ref[...] = _rms(x1, gq_ref[...]).astype(BF16)

    return pl.pallas_call(
        body, name="residual_mix", grid=(S // ROW_TILE,),
        in_specs=[_row_spec(D), _row_spec(D), _vec_spec(D), _vec_spec(D)],
        out_specs=[_row_spec(D), _row_spec(D)],
        out_shape=[jax.ShapeDtypeStruct((S, D), F32), jax.ShapeDtypeStruct((S, D), BF16)],
        compiler_params=_params(("parallel",)),
    )(x, out, g_post, g_pre)


def _loss_call(y, x1, g_post, target):
    S, D = y.shape

    def body(y_ref, x1_ref, g_ref, t_ref, loss_ref, dx_ref):
        err = x1_ref[...] + _rms(y_ref[...], g_ref[...]) - t_ref[...]
        dx_ref[...] = err * (1.0 / D)
        part = 0.5 * jnp.sum(jnp.mean(err * err, axis=-1, keepdims=True), axis=0, keepdims=True)
        _accumulate(loss_ref, jnp.broadcast_to(part, (SUBLANES, LANES)))

    return pl.pallas_call(
        body, name="residual_ffn_loss", grid=(S // ROW_TILE,),
        in_specs=[_row_spec(D), _row_spec(D), _vec_spec(D), _row_spec(D)],
        out_specs=[pl.BlockSpec((SUBLANES, LANES), lambda i: (0, 0)), _row_spec(D)],
        out_shape=[jax.ShapeDtypeStruct((SUBLANES, LANES), F32), jax.ShapeDtypeStruct((S, D), F32)],
        compiler_params=_params(("arbitrary",)),
    )(y, x1, g_post, target)


def _rms_bwd_call(x, g, dy, name):
    S, D = x.shape

    def body(x_ref, g_ref, dy_ref, dx_ref, dg_ref):
        dx, dg = _rms_bwd(x_ref[...], g_ref[...], dy_ref[...])
        dx_ref[...] = dx.astype(BF16)
        _accumulate(dg_ref, dg)

    return pl.pallas_call(
        body, name=name, grid=(S // ROW_TILE,),
        in_specs=[_row_spec(D), _vec_spec(D), _row_spec(D)],
        out_specs=[_row_spec(D), _vec_spec(D)],
        out_shape=[jax.ShapeDtypeStruct((S, D), BF16), jax.ShapeDtypeStruct((1, D), F32)],
        compiler_params=_params(("arbitrary",)),
    )(x, g, dy)


def _mid_bwd_call(x1, g_pre, dh2, dx2, out, g_post):
    S, D = x1.shape

    def body(x1_ref, gq_ref, dh_ref, dx2_ref, o_ref, gp_ref, dx1_ref, do_ref, dgq_ref, dgp_ref):
        d, dgq = _rms_bwd(x1_ref[...], gq_ref[...], dh_ref[...])
        dx1 = dx2_ref[...] + d
        dx1_ref[...] = dx1
        do, dgp = _rms_bwd(o_ref[...], gp_ref[...], dx1)
        do_ref[...] = do.astype(BF16)
        _accumulate(dgq_ref, dgq)
        _accumulate(dgp_ref, dgp)

    return pl.pallas_call(
        body, name="residual_mix_bwd", grid=(S // ROW_TILE,),
        in_specs=[_row_spec(D), _vec_spec(D), _row_spec(D), _row_spec(D), _row_spec(D), _vec_spec(D)],
        out_specs=[_row_spec(D), _row_spec(D), _vec_spec(D), _vec_spec(D)],
        out_shape=[jax.ShapeDtypeStruct((S, D), F32), jax.ShapeDtypeStruct((S, D), BF16)] + [jax.ShapeDtypeStruct((1, D), F32)] * 2,
        compiler_params=_params(("arbitrary",)),
    )(x1, g_pre, dh2, dx2, out, g_post)


def _in_bwd_call(x, g, dh1, dx1):
    S, D = x.shape

    def body(x_ref, g_ref, dh_ref, dx1_ref, gx_ref, dg_ref):
        d, dg = _rms_bwd(x_ref[...], g_ref[...], dh_ref[...])
        gx_ref[...] = dx1_ref[...] + d
        _accumulate(dg_ref, dg)

    return pl.pallas_call(
        body, name="rms_mix_pre_bwd", grid=(S // ROW_TILE,),
        in_specs=[_row_spec(D), _vec_spec(D), _row_spec(D), _row_spec(D)],
        out_specs=[_row_spec(D), _vec_spec(D)],
        out_shape=[jax.ShapeDtypeStruct((S, D), F32), jax.ShapeDtypeStruct((1, D), F32)],
        compiler_params=_params(("arbitrary",)),
    )(x, g, dh1, dx1)


def _bucket_table(dilation):
    qi = np.arange(SPAN)[:, None]
    ki = np.arange(2 * SPAN)[None, :]
    dist = np.maximum(qi + SPAN - ki, 0) * dilation
    max_exact = N_BUCKETS // 2
    d = np.maximum(dist, 1).astype(np.float64)
    large = max_exact + (np.log(d / max_exact) / math.log(MAX_DISTANCE / max_exact) * (N_BUCKETS - max_exact)).astype(np.int32)
    large = np.minimum(large, N_BUCKETS - 1)
    return np.where(dist < max_exact, dist, large).astype(np.int32)


def _bucket_tables():
    return jnp.asarray(np.stack([_bucket_table(r) for _, r in DILATED_PATTERNS]))


def _bias_table_call(rel_bias, buckets):
    def body(rb_ref, bk_ref, o_ref):
        for h in range(N_HEADS):
            bk = bk_ref[h // HEADS_PER_GROUP]

            def step(b, acc):
                return jnp.where(bk == b, rb_ref[b, h], acc)

            o_ref[h] = lax.fori_loop(0, N_BUCKETS, step, jnp.zeros((SPAN, 2 * SPAN), F32))

    return pl.pallas_call(
        body, name="rel_bias_table",
        in_specs=[pl.BlockSpec(memory_space=pltpu.SMEM), pl.BlockSpec(memory_space=pltpu.VMEM)],
        out_specs=pl.BlockSpec(memory_space=pltpu.VMEM),
        out_shape=jax.ShapeDtypeStruct((N_HEADS, SPAN, 2 * SPAN), F32),
    )(rel_bias, buckets)


def _bias_grad_call(dbias, buckets):
    def body(db_ref, bk_ref, o_ref, rows_ref):
        for h in range(N_HEADS):
            bk = bk_ref[h // HEADS_PER_GROUP]
            dv = db_ref[h]

            def step(b, carry):
                rows_ref[h, b] = jnp.sum(jnp.where(bk == b, dv, 0.0), axis=0, keepdims=True)
                return carry

            lax.fori_loop(0, N_BUCKETS, step, 0)
        o_ref[...] = jnp.sum(rows_ref[...], axis=-1, keepdims=True)

    out = pl.pallas_call(
        body, name="rel_bias_grad",
        in_specs=[pl.BlockSpec(memory_space=pltpu.VMEM), pl.BlockSpec(memory_space=pltpu.VMEM)],
        out_specs=pl.BlockSpec(memory_space=pltpu.VMEM),
        out_shape=jax.ShapeDtypeStruct((N_HEADS, N_BUCKETS, 1, 1), F32),
        scratch_shapes=[pltpu.VMEM((N_HEADS, N_BUCKETS, 1, 2 * SPAN), F32)],
    )(dbias, buckets)
    return out.reshape(N_HEADS, N_BUCKETS).T


def _dot_nt(a, b):
    return lax.dot_general(a, b, (((1,), (1,)), ((), ())), preferred_element_type=F32)


def _dot_nn(a, b):
    return lax.dot_general(a, b, (((1,), (0,)), ((), ())), preferred_element_type=F32)


def _dot_tn(a, b):
    return lax.dot_general(a, b, (((0,), (0,)), ((), ())), preferred_element_type=F32)


def _band_masks(n, nb):
    qi = lax.broadcasted_iota(jnp.int32, (SPAN, SPAN), 0)
    ki = lax.broadcasted_iota(jnp.int32, (SPAN, SPAN), 1)
    prev_ok = jnp.logical_and(ki >= qi, n > 0)
    cur_ok = ki <= qi
    next_ok = jnp.logical_and(ki >= qi, n < nb - 1)
    return prev_ok, cur_ok, next_ok


def _wide_band_mask(n):
    qi = lax.broadcasted_iota(jnp.int32, (SPAN, 2 * SPAN), 0)
    ki = lax.broadcasted_iota(jnp.int32, (SPAN, 2 * SPAN), 1)
    prev_ok = jnp.logical_and(jnp.logical_and(ki < SPAN, ki >= qi), n > 0)
    cur_ok = jnp.logical_and(ki >= SPAN, ki - SPAN <= qi)
    return jnp.logical_or(prev_ok, cur_ok)


def _attn_plan(S, group):
    r = DILATED_PATTERNS[group][1]
    hp, per = (HEADS_PER_GROUP, 1) if r == 1 else (2, 4)
    return r, S // (r * SPAN), hp, per


def _residue_rows(rho, r):
    return slice(None) if r == 1 else pl.ds(rho, SPAN, stride=r)


def _for_residues(r, per, fn):
    if r == per:
        for u in range(per):
            fn(u)
        return

    def step(i, carry):
        for u in range(per):
            fn(i * per + u)
        return carry

    lax.fori_loop(0, r // per, step, 0)


def _attn_fwd_call(proj, bias, group):
    S = proj.shape[0]
    r, nb, hp, per = _attn_plan(S, group)
    scale = HEAD_DIM ** -0.5
    kinds = ("q", "kp", "kc", "vp", "vc") if nb > 1 else ("q", "kc", "vc")

    def body(*refs):
        ins = {kind: refs[i * hp:(i + 1) * hp] for i, kind in enumerate(kinds)}
        b_ref, o_ref, lse_ref = refs[len(kinds) * hp:]
        n = pl.program_id(1)
        prev_ok, cur_ok, _ = _band_masks(n, nb)

        band_ok = _wide_band_mask(n) if nb > 1 else cur_ok

        def residue(rho):
            rows = _residue_rows(rho, r)
            for j in range(hp):
                get = lambda kind: ins[kind][j][rows, :].astype(BF16)
                q = get("q")
                if nb > 1:
                    keys, vals, bias_j = jnp.concatenate([get("kp"), get("kc")], axis=0), jnp.concatenate([get("vp"), get("vc")], axis=0), b_ref[j]
                else:
                    keys, vals, bias_j = get("kc"), get("vc"), b_ref[j, :, SPAN:]
                s = jnp.where(band_ok, _dot_nt(q, keys) * scale + bias_j, NEG_INF)
                m = jnp.max(s, axis=-1, keepdims=True)
                p = jnp.exp(s - m)
                den = jnp.sum(p, axis=-1, keepdims=True)
                o_ref[j, rows, :] = _dot_nn(p.astype(BF16), vals) / den
                lse_ref[j, rows, :] = jnp.broadcast_to(m + jnp.log(den), (SPAN, HEAD_DIM))

        _for_residues(r, per, residue)

    in_specs = [_head_spec(r, nb, hp, kind, group, jj) for kind in kinds for jj in range(hp)]
    in_specs.append(pl.BlockSpec((hp, SPAN, 2 * SPAN), lambda j, n: (group * (HEADS_PER_GROUP // hp) + j, 0, 0)))
    out = pl.BlockSpec((hp, r * SPAN, HEAD_DIM), lambda j, n: (j, n, 0))
    return pl.pallas_call(
        body, name=f"attn_fwd_g{group}", grid=(HEADS_PER_GROUP // hp, nb),
        in_specs=in_specs, out_specs=[out] * 2,
        out_shape=[jax.ShapeDtypeStruct((HEADS_PER_GROUP, S, HEAD_DIM), F32)] * 2,
        compiler_params=_params(("parallel", "parallel"), VMEM_LIMIT),
    )(*([proj] * (len(in_specs) - 1)), bias)


_PROJ_PART = dict(q=0, qn=0, kp=1, kc=1, vp=2, vc=2)


def _head_spec(r, nb, hp, kind, group, jj):
    if kind in _PROJ_PART:
        base = (_PROJ_PART[kind] * N_GROUPS + group) * HEADS_PER_GROUP
    else:
        base = 0
    if kind.endswith("p"):
        row = lambda n: jnp.maximum(n - 1, 0)
    elif kind.endswith("n"):
        row = lambda n: jnp.minimum(n + 1, nb - 1)
    else:
        row = lambda n: n
    return pl.BlockSpec((r * SPAN, HEAD_DIM), lambda j, n: (row(n), base + j * hp + jj))


def _attn_merge_call(parts):
    S = parts[0].shape[1]

    def body(o1, s1, o2, s2, o3, s3, a_ref, ab_ref, lse_ref):
        for j in range(HEADS_PER_GROUP):
            sl = slice(j * HEAD_DIM, (j + 1) * HEAD_DIM)
            mx = jnp.maximum(jnp.maximum(s1[j], s2[j]), s3[j])
            w1 = jnp.exp(s1[j] - mx)
            w2 = jnp.exp(s2[j] - mx)
            w3 = jnp.exp(s3[j] - mx)
            den = w1 + w2 + w3
            a = (w1 * o1[j] + w2 * o2[j] + w3 * o3[j]) / den
            a_ref[:, sl] = a
            ab_ref[:, sl] = a.astype(BF16)
            lse_ref[:, sl] = mx + jnp.log(den)

    heads = pl.BlockSpec((HEADS_PER_GROUP, ROW_TILE, HEAD_DIM), lambda i: (0, i, 0))
    return pl.pallas_call(
        body, name="attn_merge", grid=(S // ROW_TILE,),
        in_specs=[heads] * 6, out_specs=[_row_spec(GROUP_WIDTH)] * 3,
        out_shape=[jax.ShapeDtypeStruct((S, GROUP_WIDTH), F32), jax.ShapeDtypeStruct((S, GROUP_WIDTH), BF16),
                   jax.ShapeDtypeStruct((S, GROUP_WIDTH), F32)],
        compiler_params=_params(("parallel",)),
    )(*parts)


def _attn_delta_call(a, da):
    S = a.shape[0]

    def body(a_ref, da_ref, d_ref):
        for j in range(HEADS_PER_GROUP):
            sl = slice(j * HEAD_DIM, (j + 1) * HEAD_DIM)
            d = jnp.sum(a_ref[:, sl] * da_ref[:, sl], axis=-1, keepdims=True)
            d_ref[:, sl] = jnp.broadcast_to(d, (ROW_TILE, HEAD_DIM))

    return pl.pallas_call(
        body, name="attn_delta", grid=(S // ROW_TILE,),
        in_specs=[_row_spec(GROUP_WIDTH)] * 2, out_specs=_row_spec(GROUP_WIDTH),
        out_shape=jax.ShapeDtypeStruct((S, GROUP_WIDTH), F32),
        compiler_params=_params(("parallel",)),
    )(a, da)


def _attn_bwd_call(proj, bias, da, lse, delta, group):
    S = proj.shape[0]
    r, nb, hp, per = _attn_plan(S, group)
    scale = HEAD_DIM ** -0.5
    kinds = ("q", "qn", "kp", "kc", "vp", "vc", "da", "dan", "lse", "lsen", "dl", "dln") if nb > 1 else ("q", "kc", "vc", "da", "lse", "dl")
    source = dict(da=da, dan=da, lse=lse, lsen=lse, dl=delta, dln=delta)

    def body(*refs):
        ins = {kind: refs[i * hp:(i + 1) * hp] for i, kind in enumerate(kinds)}
        b_ref, dq_ref, dk_ref, dv_ref, db_ref = refs[len(kinds) * hp:]
        n = pl.program_id(1)
        prev_ok, cur_ok, next_ok = _band_masks(n, nb)

        @pl.when(n == 0)
        def _():
            db_ref[...] = jnp.zeros_like(db_ref)

        band_ok = _wide_band_mask(n) if nb > 1 else cur_ok

        def residue(rho):
            rows = _residue_rows(rho, r)
            for j in range(hp):
                get = lambda kind: ins[kind][j][rows, :]
                q = get("q").astype(BF16)
                kc = get("kc").astype(BF16)
                vc = get("vc").astype(BF16)
                dav = get("da").astype(BF16)
                lse_q, dl_q = get("lse"), get("dl")
                if nb == 1:
                    pc = jnp.exp(jnp.where(cur_ok, _dot_nt(q, kc) * scale + b_ref[j, :, SPAN:], NEG_INF) - lse_q)
                    dsc = pc * (_dot_nt(dav, vc) - dl_q)
                    dsc_b = dsc.astype(BF16)
                    dq = _dot_nn(dsc_b, kc)
                    dk = _dot_tn(dsc_b, q)
                    dv = _dot_tn(pc.astype(BF16), dav)
                    db_ref[j, :, SPAN:] += dsc
                else:
                    qn = get("qn").astype(BF16)
                    dan = get("dan").astype(BF16)
                    keys = jnp.concatenate([get("kp").astype(BF16), kc], axis=0)
                    vals = jnp.concatenate([get("vp").astype(BF16), vc], axis=0)
                    wide = lambda t: jnp.concatenate([t, t], axis=1)
                    p = jnp.exp(jnp.where(band_ok, _dot_nt(q, keys) * scale + b_ref[j], NEG_INF) - wide(lse_q))
                    ds = p * (_dot_nt(dav, vals) - wide(dl_q))
                    dq = _dot_nn(ds.astype(BF16), keys)
                    db_ref[j] += ds
                    pn = jnp.exp(jnp.where(next_ok, _dot_nt(qn, kc) * scale + b_ref[j, :, :SPAN], NEG_INF) - get("lsen"))
                    dsn = pn * (_dot_nt(dan, vc) - get("dln"))
                    both = lambda cur_part, next_part: jnp.concatenate([cur_part.astype(BF16), next_part.astype(BF16)], axis=0)
                    dk = _dot_tn(both(ds[:, SPAN:], dsn), jnp.concatenate([q, qn], axis=0))
                    dv = _dot_tn(both(p[:, SPAN:], pn), jnp.concatenate([dav, dan], axis=0))
                dq_ref[j, rows, :] = dq * scale
                dk_ref[j, rows, :] = dk * scale
                dv_ref[j, rows, :] = dv

        _for_residues(r, per, residue)

    per_group = HEADS_PER_GROUP // hp
    band = (hp, SPAN, 2 * SPAN)
    in_specs = [_head_spec(r, nb, hp, kind, group, jj) for kind in kinds for jj in range(hp)]
    in_specs.append(pl.BlockSpec(band, lambda j, n: (group * per_group + j, 0, 0)))
    operands = [source.get(kind, proj) for kind in kinds for _ in range(hp)] + [bias]
    out = pl.BlockSpec((hp, r * SPAN, HEAD_DIM), lambda j, n: (j, n, 0))
    return pl.pallas_call(
        body, name=f"attn_bwd_g{group}", grid=(per_group, nb),
        in_specs=in_specs,
        out_specs=[out] * 3 + [pl.BlockSpec(band, lambda j, n: (j, 0, 0))],
        out_shape=[jax.ShapeDtypeStruct((HEADS_PER_GROUP, S, HEAD_DIM), F32)] * 3
        + [jax.ShapeDtypeStruct((HEADS_PER_GROUP, SPAN, 2 * SPAN), F32)],
        compiler_params=_params(("parallel", "arbitrary"), VMEM_LIMIT),
    )(*operands)


def _dproj_call(dqkv, tails):
    S = tails[0].shape[0]
    width = len(dqkv) * GROUP_WIDTH + sum(t.shape[1] for t in tails)

    def body(*refs):
        o_ref = refs[-1]
        col = 0
        for ref in refs[:len(dqkv)]:
            for j in range(HEADS_PER_GROUP):
                o_ref[:, col:col + HEAD_DIM] = ref[j].astype(BF16)
                col += HEAD_DIM
        for ref in refs[len(dqkv):-1]:
            o_ref[:, col:col + ref.shape[1]] = ref[...]
            col += ref.shape[1]

    heads = pl.BlockSpec((HEADS_PER_GROUP, ROW_TILE, HEAD_DIM), lambda i: (0, i, 0))
    return pl.pallas_call(
        body, name="dproj_assemble", grid=(S // ROW_TILE,),
        in_specs=[heads] * len(dqkv) + [_row_spec(t.shape[1]) for t in tails],
        out_specs=_row_spec(width), out_shape=jax.ShapeDtypeStruct((S, width), BF16),
        compiler_params=_params(("parallel",)),
    )(*dqkv, *tails)


def _taps(width):
    return [(k, (width - 1 - k) // SUBLANES, (width - 1 - k) % SUBLANES) for k in range(width)]


def _shifted(win, width, pad, up):
    total = win.shape[0]
    for b in range(SUBLANES):
        taps = [(k, a) for k, a, bb in _taps(width) if bb == b]
        if not taps:
            continue
        if up:
            rolled = win if b == 0 else pltpu.roll(win, total - b, axis=0)
        else:
            rolled = win if b == 0 else pltpu.roll(win, b, axis=0)
        for k, a in taps:
            start = SUBLANES * a if up else pad - SUBLANES * a
            yield k, rolled[start:start + TIME_BLOCK, :]


def _conv_block(win, w_ref, width, pad):
    acc = None
    for k, rows in _shifted(win, width, pad, up=False):
        term = w_ref[k:k + 1, :] * rows
        acc = term if acc is None else acc + term
    return acc


def _conv_transpose_block(win, w_ref, width, pad):
    acc = None
    for k, rows in _shifted(win, width, pad, up=True):
        term = w_ref[k:k + 1, :] * rows
        acc = term if acc is None else acc + term
    return acc


def _conv_weight_grad(win, dy, dw_ref, width, pad):
    for k, rows in _shifted(win, width, pad, up=False):
        dw_ref[k:k + 1, :] += jnp.sum(dy * rows, axis=0, keepdims=True)


def _time_loop(S, step):
    def it(tb, carry):
        step(pl.multiple_of(tb * TIME_BLOCK, TIME_BLOCK))
        return carry

    lax.fori_loop(0, S // TIME_BLOCK, it, 0)


def _conv_fwd_call(proj, col0, w, b):
    S = proj.shape[0]
    C = w.shape[1]
    nt = C // LANES
    v0, g0 = col0 // LANES, (col0 + C) // LANES

    def body(val_ref, gate_ref, w_ref, b_ref, o_ref, pad_ref):
        pad_ref[0:CONV_PAD, :] = jnp.zeros((CONV_PAD, LANES), F32)
        pad_ref[CONV_PAD:, :] = val_ref[...] * _sigmoid(gate_ref[...])

        def step(t0):
            win = pad_ref[pl.ds(t0, TIME_BLOCK + CONV_PAD), :]
            o_ref[pl.ds(t0, TIME_BLOCK), :] = _conv_block(win, w_ref, CONV_WIDTH, CONV_PAD) + b_ref[...]

        _time_loop(S, step)

    seq = lambda off: pl.BlockSpec((S, LANES), lambda i: (0, off + i))
    return pl.pallas_call(
        body, name="conv_module", grid=(nt,),
        in_specs=[seq(v0), seq(g0), pl.BlockSpec((CONV_WIDTH, LANES), lambda i: (0, i)), pl.BlockSpec((1, LANES), lambda i: (0, i))],
        out_specs=seq(0), out_shape=jax.ShapeDtypeStruct((S, C), F32),
        scratch_shapes=[pltpu.VMEM((S + CONV_PAD, LANES), F32)],
        compiler_params=_params(("parallel",)),
    )(proj, proj, w, b)


def _conv_bwd_call(proj, col0, w, dc1):
    S = proj.shape[0]
    C = w.shape[1]
    nt = C // LANES
    v0, g0 = col0 // LANES, (col0 + C) // LANES

    def body(val_ref, gate_ref, w_ref, dy_ref, dval_ref, dgate_ref, dw_ref, db_ref, xpad_ref, dpad_ref, dwacc_ref):
        xpad_ref[0:CONV_PAD, :] = jnp.zeros((CONV_PAD, LANES), F32)
        xpad_ref[CONV_PAD:, :] = val_ref[...] * _sigmoid(gate_ref[...])
        dpad_ref[0:S, :] = dy_ref[...]
        dpad_ref[S:, :] = jnp.zeros((CONV_PAD, LANES), F32)
        dwacc_ref[...] = jnp.zeros_like(dwacc_ref)

        def step(t0):
            rows = pl.ds(t0, TIME_BLOCK)
            _conv_weight_grad(xpad_ref[pl.ds(t0, TIME_BLOCK + CONV_PAD), :], dy_ref[rows, :], dwacc_ref, CONV_WIDTH, CONV_PAD)
            dc0 = _conv_transpose_block(dpad_ref[pl.ds(t0, TIME_BLOCK + CONV_PAD), :], w_ref, CONV_WIDTH, CONV_PAD)
            sg = _sigmoid(gate_ref[rows, :])
            dval_ref[rows, :] = (dc0 * sg).astype(BF16)
            dgate_ref[rows, :] = (dc0 * val_ref[rows, :] * sg * (1.0 - sg)).astype(BF16)

        _time_loop(S, step)
        dw_ref[...] = dwacc_ref[...]
        db_ref[...] = jnp.sum(dy_ref[...], axis=0, keepdims=True)

    seq = lambda off: pl.BlockSpec((S, LANES), lambda i: (0, off + i))
    return pl.pallas_call(
        body, name="conv_module_bwd", grid=(nt,),
        in_specs=[seq(v0), seq(g0), pl.BlockSpec((CONV_WIDTH, LANES), lambda i: (0, i)), seq(0)],
        out_specs=[seq(0), seq(0), pl.BlockSpec((CONV_PAD, LANES), lambda i: (0, i)), pl.BlockSpec((1, LANES), lambda i: (0, i))],
        out_shape=[jax.ShapeDtypeStruct((S, C), BF16), jax.ShapeDtypeStruct((S, C), BF16),
                   jax.ShapeDtypeStruct((CONV_PAD, C), F32), jax.ShapeDtypeStruct((1, C), F32)],
        scratch_shapes=[pltpu.VMEM((S + CONV_PAD, LANES), F32), pltpu.VMEM((S + CONV_PAD, LANES), F32),
                        pltpu.VMEM((CONV_PAD, LANES), F32)],
        compiler_params=_params(("parallel",)),
    )(proj, proj, w, dc1)


def _ffn_fwd_call(u, w, b):
    S, C2 = u.shape
    C = C2 // 2
    nt = C // LANES

    def body(ug_ref, uv_ref, wg_ref, wv_ref, bg_ref, bv_ref, f_ref, pg_ref, pv_ref):
        zeros = jnp.zeros((FFN_PAD, LANES), F32)
        pg_ref[0:FFN_PAD, :] = zeros
        pv_ref[0:FFN_PAD, :] = zeros
        pg_ref[FFN_PAD:, :] = ug_ref[...]
        pv_ref[FFN_PAD:, :] = uv_ref[...]

        def step(t0):
            win = pl.ds(t0, TIME_BLOCK + FFN_PAD)
            cg = _conv_block(pg_ref[win, :], wg_ref, FFN_CONV_WIDTH, FFN_PAD) + bg_ref[...]
            cv = _conv_block(pv_ref[win, :], wv_ref, FFN_CONV_WIDTH, FFN_PAD) + bv_ref[...]
            f_ref[pl.ds(t0, TIME_BLOCK), :] = (_gelu(cg) * cv).astype(BF16)

        _time_loop(S, step)

    seq = lambda off: pl.BlockSpec((S, LANES), lambda i: (0, off + i))
    wsp = lambda off: pl.BlockSpec((FFN_CONV_WIDTH, LANES), lambda i: (0, off + i))
    bsp = lambda off: pl.BlockSpec((1, LANES), lambda i: (0, off + i))
    return pl.pallas_call(
        body, name="ffn_conv_geglu", grid=(nt,),
        in_specs=[seq(0), seq(nt), wsp(0), wsp(nt), bsp(0), bsp(nt)],
        out_specs=seq(0), out_shape=jax.ShapeDtypeStruct((S, C), BF16),
        scratch_shapes=[pltpu.VMEM((S + FFN_PAD, LANES), F32)] * 2,
        compiler_params=_params(("parallel",)),
    )(u, u, w, w, b, b)


def _ffn_bwd_call(u, w, b, df):
    S, C2 = u.shape
    C = C2 // 2
    nt = C // LANES

    def body(ug_ref, uv_ref, wg_ref, wv_ref, bg_ref, bv_ref, df_ref,
             dug_ref, duv_ref, dwg_ref, dwv_ref, dbg_ref, dbv_ref,
             pg_ref, pv_ref, dg_ref, dv_ref, dwg_acc, dwv_acc, dbg_acc, dbv_acc):
        zeros = jnp.zeros((FFN_PAD, LANES), F32)
        pg_ref[0:FFN_PAD, :] = zeros
        pv_ref[0:FFN_PAD, :] = zeros
        pg_ref[FFN_PAD:, :] = ug_ref[...]
        pv_ref[FFN_PAD:, :] = uv_ref[...]
        dg_ref[S:, :] = zeros
        dv_ref[S:, :] = zeros
        dwg_acc[...] = jnp.zeros_like(dwg_acc)
        dwv_acc[...] = jnp.zeros_like(dwv_acc)
        dbg_acc[...] = jnp.zeros_like(dbg_acc)
        dbv_acc[...] = jnp.zeros_like(dbv_acc)

        def first(t0):
            win = pl.ds(t0, TIME_BLOCK + FFN_PAD)
            rows = pl.ds(t0, TIME_BLOCK)
            xg = pg_ref[win, :]
            xv = pv_ref[win, :]
            cg = _conv_block(xg, wg_ref, FFN_CONV_WIDTH, FFN_PAD) + bg_ref[...]
            cv = _conv_block(xv, wv_ref, FFN_CONV_WIDTH, FFN_PAD) + bv_ref[...]
            dfb = df_ref[rows, :]
            dcg = dfb * cv * _gelu_grad(cg)
            dcv = dfb * _gelu(cg)
            dg_ref[rows, :] = dcg
            dv_ref[rows, :] = dcv
            _conv_weight_grad(xg, dcg, dwg_acc, FFN_CONV_WIDTH, FFN_PAD)
            _conv_weight_grad(xv, dcv, dwv_acc, FFN_CONV_WIDTH, FFN_PAD)
            dbg_acc[...] += jnp.sum(dcg, axis=0, keepdims=True)
            dbv_acc[...] += jnp.sum(dcv, axis=0, keepdims=True)

        def second(t0):
            win = pl.ds(t0, TIME_BLOCK + FFN_PAD)
            rows = pl.ds(t0, TIME_BLOCK)
            dug_ref[rows, :] = _conv_transpose_block(dg_ref[win, :], wg_ref, FFN_CONV_WIDTH, FFN_PAD).astype(BF16)
            duv_ref[rows, :] = _conv_transpose_block(dv_ref[win, :], wv_ref, FFN_CONV_WIDTH, FFN_PAD).astype(BF16)

        _time_loop(S, first)
        _time_loop(S, second)
        dwg_ref[...] = dwg_acc[...]
        dwv_ref[...] = dwv_acc[...]
        dbg_ref[...] = dbg_acc[...]
        dbv_ref[...] = dbv_acc[...]

    seq = lambda off: pl.BlockSpec((S, LANES), lambda i: (0, off + i))
    wsp = lambda off: pl.BlockSpec((FFN_CONV_WIDTH, LANES), lambda i: (0, off + i))
    bsp = lambda off: pl.BlockSpec((1, LANES), lambda i: (0, off + i))
    return pl.pallas_call(
        body, name="ffn_conv_geglu_bwd", grid=(nt,),
        in_specs=[seq(0), seq(nt), wsp(0), wsp(nt), bsp(0), bsp(nt), seq(0)],
        out_specs=[seq(0), seq(0), pl.BlockSpec((SUBLANES, LANES), lambda i: (0, i)), pl.BlockSpec((SUBLANES, LANES), lambda i: (0, i)),
                   bsp(0), bsp(0)],
        out_shape=[jax.ShapeDtypeStruct((S, C), BF16)] * 2 + [jax.ShapeDtypeStruct((SUBLANES, C), F32)] * 2
        + [jax.ShapeDtypeStruct((1, C), F32)] * 2,
        scratch_shapes=[pltpu.VMEM((S + FFN_PAD, LANES), F32)] * 4 + [pltpu.VMEM((SUBLANES, LANES), F32)] * 2
        + [pltpu.VMEM((1, LANES), F32)] * 2,
        compiler_params=_params(("parallel",)),
    )(u, u, w, w, b, b, df)


def _adamw_call(w, g, m, v, name):
    R, C = w.shape
    tr = _row_tile(R, C)
    c1 = 1.0 / (1.0 - ADAM_B1 ** ADAM_STEP)
    c2 = 1.0 / (1.0 - ADAM_B2 ** ADAM_STEP)

    def body(w_ref, g_ref, m_ref, v_ref, d_ref, mo_ref, vo_ref):
        gv = g_ref[...]
        mn = ADAM_B1 * m_ref[...] + (1.0 - ADAM_B1) * gv
        vn = ADAM_B2 * v_ref[...] + (1.0 - ADAM_B2) * (gv * gv)
        mo_ref[...] = mn
        vo_ref[...] = vn
        d_ref[...] = -ADAM_LR * ((mn * c1) / (jnp.sqrt(vn * c2) + ADAM_EPS) + ADAM_WD * w_ref[...])

    spec = pl.BlockSpec((tr, C), lambda i: (i, 0))
    return pl.pallas_call(
        body, name=name, grid=(R // tr,),
        in_specs=[spec] * 4, out_specs=[spec] * 3,
        out_shape=[jax.ShapeDtypeStruct((R, C), F32)] * 3,
        compiler_params=_params(("parallel",)),
    )(w, g, m, v)


def _position():
    return lax.axis_index("x"), lax.axis_index("y"), lax.axis_index("c")


def _chip_peers(x, y):
    return [(x, 1 - y), (1 - x, y), (1 - x, 1 - y)]


def _half_rows(ref, core, rows):
    h = rows // 2
    start = pl.multiple_of(core * h, 16)
    return ref.at[pl.ds(start, h), :] if len(ref.shape) == 2 else ref.at[:, pl.ds(start, h), :]


def _shard_half(ref, shard, core, rows):
    h = rows // 2
    return ref.at[shard, pl.ds(pl.multiple_of(core * h, 16), h), :]


ANY = pl.BlockSpec(memory_space=pl.ANY)


def _allgather_call(shards, whole):
    n, nw = len(shards), len(whole)
    outs_shape = [jax.ShapeDtypeStruct((N_CHIPS,) + s.shape, s.dtype) for s in shards + whole]

    def body(*refs):
        ins, outs = refs[:n + nw], refs[n + nw:2 * (n + nw)]
        send_sems, recv_sems, pass_send, pass_recv, own_send, own_recv = refs[2 * (n + nw):]
        x, y, c = _position()
        chip = 2 * x + y
        peers = _chip_peers(x, y)
        sent, local = [], []
        for i in range(n + nw):
            cp = pltpu.make_async_remote_copy(src_ref=ins[i], dst_ref=outs[i].at[chip], send_sem=own_send.at[i],
                                              recv_sem=own_recv.at[i], device_id=(x, y, 1 - c), device_id_type=MESH)
            cp.start()
            local.append(cp)
            rows = ins[i].shape[0]
            for k, (px, py) in enumerate(peers):
                if i < n:
                    src, dst = _half_rows(ins[i], c, rows), _shard_half(outs[i], chip, c, rows)
                else:
                    src, dst = ins[i], outs[i].at[chip]
                cp = pltpu.make_async_remote_copy(src_ref=src, dst_ref=dst, send_sem=send_sems.at[i, k],
                                                  recv_sem=recv_sems.at[i, k], device_id=(px, py, c), device_id_type=MESH)
                cp.start()
                sent.append(cp)
        passed = []
        for i in range(n + nw):
            rows = ins[i].shape[0]
            for k, (px, py) in enumerate(peers):
                landed = _shard_half(outs[i], 2 * px + py, c, rows) if i < n else outs[i].at[2 * px + py]
                pltpu.make_async_remote_copy(src_ref=landed, dst_ref=landed, send_sem=send_sems.at[i, k],
                                             recv_sem=recv_sems.at[i, k], device_id=(px, py, c), device_id_type=MESH).wait_recv()
                if i < n:
                    cp = pltpu.make_async_remote_copy(src_ref=landed, dst_ref=landed, send_sem=pass_send.at[i, k],
                                                      recv_sem=pass_recv.at[i, k], device_id=(x, y, 1 - c), device_id_type=MESH)
                    cp.start()
                    passed.append(cp)
        for cp in sent:
            cp.wait_send()
        for cp in passed:
            cp.wait()
        for cp in local:
            cp.wait()

    return pl.pallas_call(
        body, name="weight_allgather",
        in_specs=[ANY] * (n + nw), out_specs=[ANY] * (n + nw), out_shape=outs_shape,
        scratch_shapes=[pltpu.SemaphoreType.DMA((n + nw, 3)), pltpu.SemaphoreType.DMA((n + nw, 3)),
                        pltpu.SemaphoreType.DMA((n, 3)), pltpu.SemaphoreType.DMA((n, 3)),
                        pltpu.SemaphoreType.DMA((n + nw,)), pltpu.SemaphoreType.DMA((n + nw,))],
    )(*shards, *whole)


HBM_SPEC = pl.BlockSpec(memory_space=pltpu.HBM)
SEM_SPEC = pl.BlockSpec(memory_space=pltpu.SEMAPHORE)
DATAFLOW = pltpu.SideEffectType.DATAFLOW_SIDE_EFFECTING


def _in_hbm(a):
    return pltpu.with_memory_space_constraint(a, pltpu.HBM)


def _split_start(name, srcs, lands, n_sems, copies, after):
    n, m = len(srcs), len(lands)

    def body(*refs):
        src_refs, land_refs = refs[:n], refs[n:n + m]
        send_sem, recv_sem = refs[n + m + 1], refs[n + m + 2]
        token = refs[-1]
        for src, dst, dev, idx in copies(src_refs, land_refs):
            pltpu.make_async_remote_copy(src_ref=src, dst_ref=dst, send_sem=send_sem.at[idx], recv_sem=recv_sem.at[idx],
                                         device_id=dev, device_id_type=MESH).start()
        token[...] = jnp.zeros_like(token)

    outs = pl.pallas_call(
        body, name=name,
        in_specs=[HBM_SPEC] * (n + m) + [ANY],
        out_specs=[SEM_SPEC, SEM_SPEC] + [HBM_SPEC] * (n + m) + [pl.BlockSpec(memory_space=pltpu.VMEM)],
        out_shape=[pltpu.SemaphoreType.DMA((n_sems,)), pltpu.SemaphoreType.DMA((n_sems,))]
        + [pltpu.HBM(a.shape, a.dtype) for a in list(srcs) + list(lands)] + [jax.ShapeDtypeStruct((SUBLANES, LANES), F32)],
        input_output_aliases={i: 2 + i for i in range(n + m)},
        compiler_params=pltpu.CompilerParams(has_side_effects=DATAFLOW),
    )(*[_in_hbm(a) for a in list(srcs) + list(lands)], after)
    return dict(send=outs[0], recv=outs[1], srcs=list(outs[2:2 + n]), lands=list(outs[2 + n:2 + n + m]),
                tile=outs[-1], token=outs[-1][0, 0])


def _split_wait(name, started, copies, after):
    n, m = len(started["srcs"]), len(started["lands"])

    def body(*refs):
        src_refs, land_refs = refs[:n], refs[n:n + m]
        send_sem, recv_sem = refs[n + m], refs[n + m + 1]
        for src, dst, dev, idx in copies(src_refs, land_refs):
            cp = pltpu.make_async_remote_copy(src_ref=src, dst_ref=dst, send_sem=send_sem.at[idx], recv_sem=recv_sem.at[idx],
                                              device_id=dev, device_id_type=MESH)
            cp.wait_send()
            cp.wait_recv()

    arrays = started["srcs"] + started["lands"]
    outs = pl.pallas_call(
        body, name=name,
        in_specs=[HBM_SPEC] * (n + m) + [SEM_SPEC, SEM_SPEC, ANY],
        out_specs=[HBM_SPEC] * (n + m),
        out_shape=[pltpu.HBM(a.shape, a.dtype) for a in arrays],
        input_output_aliases={i: i for i in range(n + m)},
        compiler_params=pltpu.CompilerParams(has_side_effects=DATAFLOW),
    )(*arrays, started["send"], started["recv"], after)
    return list(outs)


def _gather_copies(srcs, lands):
    x, y, c = _position()
    chip = 2 * x + y
    targets = [(px, py, c) for px, py in _chip_peers(x, y)] + [(x, y, 1 - c)]
    return [(s, l.at[chip], dev, len(targets) * i + k) for i, (s, l) in enumerate(zip(srcs, lands)) for k, dev in enumerate(targets)]


def _sibling_copies(srcs, lands):
    x, y, c = _position()
    return [(_half_rows(srcs[0], 1 - c, srcs[0].shape[1]), lands[0], (x, y, 1 - c), 0)]


def _exchange_copies(srcs, lands):
    x, y, c = _position()
    return [(srcs[0].at[2 * px + py], lands[0].at[k], (px, py, c), k) for k, (px, py) in enumerate(_chip_peers(x, y))]


def _pair_sum_call(grad, recv, core, name):
    _, h, B = recv.shape
    tr = _row_tile(h, B)

    def body(core_ref, g_ref, r_ref, o_ref, ob_ref):
        s = g_ref[...] + r_ref[...]
        o_ref[...] = s
        ob_ref[...] = s.astype(BF16)

    g_spec = pl.BlockSpec((None, tr, B), lambda q, i, core_ref: (q, core_ref[0] * (h // tr) + i, 0))
    spec = pl.BlockSpec((None, tr, B), lambda q, i, core_ref: (q, i, 0))
    return pl.pallas_call(
        body, name=name,
        grid_spec=pltpu.PrefetchScalarGridSpec(num_scalar_prefetch=1, grid=(N_CHIPS, h // tr), in_specs=[g_spec, spec],
                                               out_specs=[spec, spec]),
        out_shape=[jax.ShapeDtypeStruct(recv.shape, F32), jax.ShapeDtypeStruct(recv.shape, BF16)],
        compiler_params=_params(("parallel", "parallel")),
    )(core, grad, recv)


def _chip_sum_call(partial, recv, chip_core, name):
    _, h, B = recv.shape
    tr = _row_tile(h, B)

    def body(cc_ref, p_ref, r_ref, o_ref):
        o_ref[...] = ((p_ref[...] + r_ref[0].astype(F32)) + r_ref[1].astype(F32)) + r_ref[2].astype(F32)

    return pl.pallas_call(
        body, name=name,
        grid_spec=pltpu.PrefetchScalarGridSpec(
            num_scalar_prefetch=1, grid=(h // tr,),
            in_specs=[pl.BlockSpec((None, tr, B), lambda i, cc_ref: (cc_ref[0], i, 0)),
                      pl.BlockSpec((3, tr, B), lambda i, cc_ref: (0, i, 0))],
            out_specs=pl.BlockSpec((tr, B), lambda i, cc_ref: (cc_ref[1] * (h // tr) + i, 0))),
        out_shape=jax.ShapeDtypeStruct((2 * h, B), F32),
        compiler_params=_params(("parallel",)),
    )(chip_core, partial, recv)


def _sibling_assemble_call(shards, name="grad_sibling_assemble"):
    n = len(shards)

    def body(*refs):
        ins, outs = refs[:n], refs[n:2 * n]
        send_sems, recv_sems = refs[2 * n:]
        x, y, c = _position()
        copies = []
        for i in range(n):
            rows = shards[i].shape[0]
            cp = pltpu.make_async_remote_copy(src_ref=_half_rows(ins[i], c, rows), dst_ref=_half_rows(outs[i], c, rows),
                                              send_sem=send_sems.at[i], recv_sem=recv_sems.at[i],
                                              device_id=(x, y, 1 - c), device_id_type=MESH)
            cp.start()
            copies.append(cp)
        for cp in copies:
            cp.wait()

    return pl.pallas_call(
        body, name=name,
        in_specs=[ANY] * n, out_specs=[ANY] * n,
        out_shape=[jax.ShapeDtypeStruct(s.shape, F32) for s in shards],
        input_output_aliases={i: i for i in range(n)},
        scratch_shapes=[pltpu.SemaphoreType.DMA((n,)), pltpu.SemaphoreType.DMA((n,))],
    )(*shards)


def _small_allreduce_call(packed):
    rows = packed.shape[0]

    def body(x_ref, o_ref, buf_ref, send_sems, recv_sems):
        x, y, c = _position()
        me = 4 * x + 2 * y + c
        buf_ref[me] = x_ref[...]
        copies = []
        for k in range(1, 8):
            peer = (1 - x if k & 4 else x, 1 - y if k & 2 else y, 1 - c if k & 1 else c)
            cp = pltpu.make_async_remote_copy(src_ref=buf_ref.at[me], dst_ref=buf_ref.at[me], send_sem=send_sems.at[k - 1],
                                              recv_sem=recv_sems.at[k - 1], device_id=peer, device_id_type=MESH)
            cp.start()
            copies.append(cp)
        for cp in copies:
            cp.wait()
        acc = buf_ref[0]
        for d in range(1, 8):
            acc = acc + buf_ref[d]
        o_ref[...] = acc

    return pl.pallas_call(
        body, name="small_grad_allreduce",
        in_specs=[pl.BlockSpec(memory_space=pltpu.VMEM)], out_specs=pl.BlockSpec(memory_space=pltpu.VMEM),
        out_shape=jax.ShapeDtypeStruct((rows, LANES), F32),
        scratch_shapes=[pltpu.VMEM((8, rows, LANES), F32), pltpu.SemaphoreType.DMA((7,)), pltpu.SemaphoreType.DMA((7,))],
    )(packed)


def _pack(arrays):
    flat = jnp.concatenate([a.reshape(-1).astype(F32) for a in arrays])
    rows = -(-flat.shape[0] // LANES)
    rows = -(-rows // SUBLANES) * SUBLANES
    flat = jnp.pad(flat, (0, rows * LANES - flat.shape[0]))
    return flat.reshape(rows, LANES)


def _unpack(packed, shapes):
    flat = packed.reshape(-1)
    out, off = [], 0
    for shp in shapes:
        size = int(np.prod(shp))
        out.append(flat[off:off + size].reshape(shp))
        off += size
    return out


def _local_step(xs, target, P, late_weights, on_grad):
    S, D = xs.shape
    qkv_width = 3 * N_HEADS * HEAD_DIM
    glu_col0, gate_col0 = qkv_width, qkv_width + 2 * D
    shard_major = lambda g: g.reshape(N_CHIPS, g.shape[0] // N_CHIPS, g.shape[1])

    h1 = _rms_fwd_call(xs, P["norm_mix_pre"])
    proj = _matmul(h1, P["w_in"], "nn", "proj_in")
    buckets = _bucket_tables()
    bias = _bias_table_call(P["rel_bias"], buckets)
    parts = []
    for g in range(N_GROUPS):
        parts += _attn_fwd_call(proj, bias, g)
    a, a_bf, lse = _attn_merge_call(parts)
    P = dict(P, **late_weights("mix", a_bf))
    y_a = _matmul(a_bf, P["w_attn_out"], "nn", "attn_out")
    c1 = _conv_fwd_call(proj, glu_col0, P["conv_dw_w"], P["conv_dw_b"])
    cact = _ln_silu_call(c1, P["conv_ln_g"], P["conv_ln_b"])
    y_c = _matmul(cact, P["conv_pw_w"], "nn", "conv_pw")
    mixed = _mix_call(proj, gate_col0, P["b_gate"], y_a, y_c)
    out = _matmul(mixed, P["w_out"], "nn", "mix_out")
    x1, h2 = _res1_call(xs, out, P["norm_mix_post"], P["norm_ffn_pre"])
    P = dict(P, **late_weights("ffn", h2))
    u = _matmul(h2, P["w_up"], "nn", "ffn_up")
    f = _ffn_fwd_call(u, P["ffn_conv_w"], P["ffn_conv_b"])
    yff = _matmul(f, P["w_down"], "nn", "ffn_down")
    loss_tile, dx2 = _loss_call(yff, x1, P["norm_ffn_post"], target)

    G = {}
    dyff, G["norm_ffn_post"] = _rms_bwd_call(yff, P["norm_ffn_post"], dx2, "rms_ffn_post_bwd")
    zero = on_grad("w_down", shard_major(_matmul(f, dyff, "tn", "ffn_down_dw")))
    df = _matmul(dyff, P["w_down"], "nt", "ffn_down_dx")
    dug, duv, dwg, dwv, dbg, dbv = _ffn_bwd_call(u, P["ffn_conv_w"], P["ffn_conv_b"] + zero, df)
    G["ffn_conv_w"] = jnp.concatenate([dwg[:FFN_CONV_WIDTH], dwv[:FFN_CONV_WIDTH]], axis=1)
    G["ffn_conv_b"] = jnp.concatenate([dbg, dbv], axis=1)
    du = jnp.concatenate([dug, duv], axis=1)
    zero = on_grad("w_up", _matmul(h2, du, "tn", "ffn_up_dw", out_shards=True))
    dh2 = _matmul(du, P["w_up"], "nt", "ffn_up_dx")
    dx1, dout, G["norm_ffn_pre"], G["norm_mix_post"] = _mid_bwd_call(x1, P["norm_ffn_pre"] + zero, dh2, dx2, out, P["norm_mix_post"])
    zero = on_grad("w_out", shard_major(_matmul(mixed, dout, "tn", "mix_out_dw")))
    dmixed = _matmul(dout, P["w_out"], "nt", "mix_out_dx")
    dya, dyc, dga, dgc, dba, dbc = _mix_bwd_call(dmixed, proj, gate_col0, P["b_gate"] + zero, y_a, y_c)
    G["b_gate"] = jnp.concatenate([dba, dbc], axis=1)
    zero = on_grad("w_attn_out", _matmul(a_bf, dya, "tn", "attn_out_dw", out_shards=True))
    zero = zero + on_grad("conv_pw_w", shard_major(_matmul(cact, dyc, "tn", "conv_pw_dw")))
    da = _matmul(dya, P["w_attn_out"], "nt", "attn_out_dx")
    dcact = _matmul(dyc, P["conv_pw_w"], "nt", "conv_pw_dx")
    dc1, G["conv_ln_g"], G["conv_ln_b"] = _ln_silu_bwd_call(c1, P["conv_ln_g"] + zero, P["conv_ln_b"], dcact)
    dval, dgate, dw_dw, G["conv_dw_b"] = _conv_bwd_call(proj, glu_col0, P["conv_dw_w"], dc1)
    G["conv_dw_w"] = dw_dw[:CONV_WIDTH]
    delta = _attn_delta_call(a, da)
    dqs, dks, dvs, dbs = [], [], [], []
    for g in range(N_GROUPS):
        dq, dk, dv, db = _attn_bwd_call(proj, bias, da, lse, delta, g)
        dqs.append(dq)
        dks.append(dk)
        dvs.append(dv)
        dbs.append(db)
    G["rel_bias"] = _bias_grad_call(jnp.concatenate(dbs, axis=0), buckets)
    dproj = _dproj_call(dqs + dks + dvs, [dval, dgate, dga, dgc])
    zero = on_grad("w_in", _matmul(h1, dproj, "tn", "proj_in_dw", out_shards=True, tm=512))
    dh1 = _matmul(dproj, P["w_in"], "nt", "proj_in_dx")
    grad_x, G["norm_mix_pre"] = _in_bwd_call(xs, P["norm_mix_pre"] + zero, dh1, dx1)
    return loss_tile, grad_x, G


def kernel(x, w_in, b_gate, rel_bias, w_attn_out, conv_dw_w, conv_dw_b, conv_ln_g, conv_ln_b, conv_pw_w, w_out, norm_mix_pre, norm_mix_post, norm_ffn_pre, norm_ffn_post, w_up, ffn_conv_w, ffn_conv_b, w_down, loss_target, m_w_in, m_b_gate, m_rel_bias, m_w_attn_out, m_conv_dw_w, m_conv_dw_b, m_conv_ln_g, m_conv_ln_b, m_conv_pw_w, m_w_out, m_norm_mix_pre, m_norm_mix_post, m_norm_ffn_pre, m_norm_ffn_post, m_w_up, m_ffn_conv_w, m_ffn_conv_b, m_w_down, v_w_in, v_b_gate, v_rel_bias, v_w_attn_out, v_conv_dw_w, v_conv_dw_b, v_conv_ln_g, v_conv_ln_b, v_conv_pw_w, v_w_out, v_norm_mix_pre, v_norm_mix_post, v_norm_ffn_pre, v_norm_ffn_post, v_w_up, v_ffn_conv_w, v_ffn_conv_b, v_w_down):
    weights = dict(w_in=w_in, b_gate=b_gate, rel_bias=rel_bias, w_attn_out=w_attn_out, conv_dw_w=conv_dw_w, conv_dw_b=conv_dw_b,
                   conv_ln_g=conv_ln_g, conv_ln_b=conv_ln_b, conv_pw_w=conv_pw_w, w_out=w_out, norm_mix_pre=norm_mix_pre,
                   norm_mix_post=norm_mix_post, norm_ffn_pre=norm_ffn_pre, norm_ffn_post=norm_ffn_post, w_up=w_up,
                   ffn_conv_w=ffn_conv_w, ffn_conv_b=ffn_conv_b, w_down=w_down)
    m_in = dict(w_in=m_w_in, b_gate=m_b_gate, rel_bias=m_rel_bias, w_attn_out=m_w_attn_out, conv_dw_w=m_conv_dw_w,
                conv_dw_b=m_conv_dw_b, conv_ln_g=m_conv_ln_g, conv_ln_b=m_conv_ln_b, conv_pw_w=m_conv_pw_w, w_out=m_w_out,
                norm_mix_pre=m_norm_mix_pre, norm_mix_post=m_norm_mix_post, norm_ffn_pre=m_norm_ffn_pre,
                norm_ffn_post=m_norm_ffn_post, w_up=m_w_up, ffn_conv_w=m_ffn_conv_w, ffn_conv_b=m_ffn_conv_b, w_down=m_w_down)
    v_in = dict(w_in=v_w_in, b_gate=v_b_gate, rel_bias=v_rel_bias, w_attn_out=v_w_attn_out, conv_dw_w=v_conv_dw_w,
                conv_dw_b=v_conv_dw_b, conv_ln_g=v_conv_ln_g, conv_ln_b=v_conv_ln_b, conv_pw_w=v_conv_pw_w, w_out=v_w_out,
                norm_mix_pre=v_norm_mix_pre, norm_mix_post=v_norm_mix_post, norm_ffn_pre=v_norm_ffn_pre,
                norm_ffn_post=v_norm_ffn_post, w_up=v_w_up, ffn_conv_w=v_ffn_conv_w, ffn_conv_b=v_ffn_conv_b, w_down=v_w_down)
    names = list(weights)
    xi, yi, ci = _position()
    chip = 2 * xi + yi
    core_arr = jnp.reshape(ci, (1,)).astype(jnp.int32)

    xs = x[0]
    target = loss_target[0]
    S, D = xs.shape

    big = ["w_in", "w_attn_out", "conv_pw_w", "w_out", "w_up", "w_down"]
    row_sharded = ("conv_pw_w", "w_out", "w_down")
    bf16_shard = {k: weights[k][0].astype(BF16) for k in big}
    natural = lambda k, g: g.reshape(-1, g.shape[2]) if k in row_sharded else g
    w_in_full, dw4, fc4 = _allgather_call([bf16_shard["w_in"]], [conv_dw_w[0], ffn_conv_w[0]])
    late_sets = dict(mix=["w_attn_out", "conv_pw_w", "w_out"], ffn=["w_up", "w_down"])
    started, after = {}, w_in_full
    for tag, keys in late_sets.items():
        srcs = [bf16_shard[k] for k in keys]
        lands = [lax.empty((N_CHIPS,) + s.shape, BF16) for s in srcs]
        started[tag] = _split_start(f"gather_{tag}_start", srcs, lands, 4 * len(keys), _gather_copies, after)
        after = started[tag]["tile"]
    launched = started["mix"]["token"] + started["ffn"]["token"]

    def late_weights(tag, after):
        landed = _split_wait(f"gather_{tag}_wait", started[tag], _gather_copies, after)[len(late_sets[tag]):]
        return {k: natural(k, g) for k, g in zip(late_sets[tag], landed)}

    chip_core = jnp.stack([chip, ci]).astype(jnp.int32)
    exchanging, pending = {}, {}

    def pair_up(after):
        token = jnp.float32(0.0)
        for k in list(exchanging):
            g3, r1 = _split_wait(f"sibling_exchange_wait_{k}", exchanging.pop(k), _sibling_copies, after)
            s32, s16 = _pair_sum_call(g3, r1, core_arr, f"pair_sum_{k}")
            land = lax.empty((3,) + s16.shape[1:], BF16)
            pending[k] = (s32, _split_start(f"chip_exchange_start_{k}", [s16], [land], 3, _exchange_copies, s32))
            token = token + pending[k][1]["token"]
        return token

    def on_grad(k, g3):
        token = pair_up(g3[0, :SUBLANES, :LANES])
        land = lax.empty((N_CHIPS, g3.shape[1] // 2, g3.shape[2]), F32)
        exchanging[k] = _split_start(f"sibling_exchange_start_{k}", [g3], [land], 1, _sibling_copies, core_arr)
        return token + exchanging[k]["token"]

    def finish(keys, after, tag):
        halves = []
        for k in keys:
            s32, st = pending[k]
            recv2 = _split_wait(f"chip_exchange_wait_{k}", st, _exchange_copies, after)[1]
            halves.append(_chip_sum_call(s32, recv2, chip_core, f"chip_sum_{k}"))
        return dict(zip(keys, _sibling_assemble_call(halves, f"grad_sibling_assemble_{tag}")))

    P = dict(w_in=w_in_full, conv_dw_w=jnp.concatenate(list(dw4), axis=1), ffn_conv_w=jnp.concatenate(list(fc4), axis=1),
             b_gate=b_gate, rel_bias=rel_bias, conv_dw_b=conv_dw_b, conv_ln_g=conv_ln_g, conv_ln_b=conv_ln_b,
             norm_mix_pre=norm_mix_pre + launched, norm_mix_post=norm_mix_post, norm_ffn_pre=norm_ffn_pre,
             norm_ffn_post=norm_ffn_post, ffn_conv_b=ffn_conv_b)
    loss_tile, grad_x, G = _local_step(xs, target, P, late_weights, on_grad)

    small = [k for k in names if k not in big]
    packed = _pack([loss_tile[:1]] + [G[k] for k in small])
    summed_block = _small_allreduce_call(packed)
    summed_block = summed_block + pair_up(summed_block[:SUBLANES])
    loss_row, *summed = _unpack(summed_block, [(1, LANES)] + [G[k].shape for k in small])
    loss = loss_row[0, 0]
    reduced = {}
    for k, gsum in zip(small, summed):
        if k in ("conv_dw_w", "ffn_conv_w"):
            cols = weights[k].shape[2]
            reduced[k] = lax.dynamic_slice_in_dim(gsum, chip * cols, cols, axis=1)
        else:
            reduced[k] = gsum

    grads, deltas, new_m, new_v = {}, {}, {}, {}

    def update(keys):
        for k in keys:
            d, mn, vn = _adamw_call(weights[k][0], reduced[k], m_in[k][0], v_in[k][0], f"adamw_{k}")
            grads[k], deltas[k], new_m[k], new_v[k] = reduced[k][None], d[None], mn[None], vn[None]

    others = [k for k in big if k != "w_in"]
    reduced.update(finish(others, summed_block, "others"))
    update(others)
    reduced.update(finish(["w_in"], deltas["w_up"], "w_in"))
    update(["w_in"])
    flat2 = lambda t: t.reshape(-1, t.shape[-1]) if t.ndim == 3 else t
    pw = _pack([flat2(weights[k]) for k in small])
    pg = _pack([reduced[k] for k in small])
    pm = _pack([flat2(m_in[k]) for k in small])
    pv = _pack([flat2(v_in[k]) for k in small])
    pd, pmn, pvn = _adamw_call(pw, pg, pm, pv, "adamw_small")
    shapes = [weights[k].shape for k in small]
    for k, gk, dk_, mk, vk in zip(small, [reduced[k] for k in small], _unpack(pd, shapes), _unpack(pmn, shapes), _unpack(pvn, shapes)):
        grads[k], deltas[k], new_m[k], new_v[k] = gk.reshape(weights[k].shape), dk_, mk, vk

    return (loss, grad_x[None], *[grads[k] for k in names], *[deltas[k] for k in names],
            *[new_m[k] for k in names], *[new_v[k] for k in names])
```

```python
import functools
import math

import jax
import jax.numpy as jnp
import numpy as np
from jax import lax
from jax.experimental import pallas as pl
from jax.experimental.pallas import tpu as pltpu

F32 = jnp.float32
BF16 = jnp.bfloat16
MESH = pl.DeviceIdType.MESH

HEAD_DIM = 128
HEADS_PER_GROUP = 4
DILATED_PATTERNS = ((128, 1), (512, 4), (2048, 16))
N_GROUPS = 3
N_HEADS = N_GROUPS * HEADS_PER_GROUP
SPAN = 128
GROUP_WIDTH = HEADS_PER_GROUP * HEAD_DIM
CONV_WIDTH = 31
FFN_CONV_WIDTH = 3
N_BUCKETS = 32
MAX_DISTANCE = 2048
RMS_EPS = 1e-6
LN_EPS = 1e-5
NEG_INF = -1e30
ADAM_LR = 0.001
ADAM_B1 = 0.9
ADAM_B2 = 0.999
ADAM_EPS = 1e-08
ADAM_WD = 0.01
ADAM_STEP = 10

LANES = 128
SUBLANES = 8
ROW_TILE = 256
TIME_BLOCK = 128
CONV_PAD = 32
FFN_PAD = 8
VMEM_LIMIT = 56 << 20


def _params(sem=None, vmem=None):
    kw = {}
    if sem is not None:
        kw["dimension_semantics"] = sem
    if vmem is not None:
        kw["vmem_limit_bytes"] = vmem
    return pltpu.CompilerParams(**kw)


def _pick(n, cands):
    for c in cands:
        if n % c == 0:
            return c
    return n


ELEMENTWISE_TILE_BYTES = 3 << 19


def _row_tile(rows, cols):
    for align in (16, SUBLANES):
        fits = [t for t in range(align, rows + 1, align) if rows % t == 0 and t * cols * 4 <= ELEMENTWISE_TILE_BYTES]
        if fits:
            return max(fits)
    return SUBLANES


N_CHIPS = 4
M_TILES = (1024, 1408, 512, 256, 128)
N_TILES = (1024, 512, 1408, 256, 128)
K_TILES = (2176, 2048, 1408, 1024, 512, 256, 128)


def _matmul(a, b, mode, name, out_shards=False, tm=None):
    assert a.dtype == BF16 and b.dtype == BF16, (name, a.dtype, b.dtype)
    b3 = b.ndim == 3
    tn = tk = None
    if mode == "nn":
        M, K = a.shape
        N = b.shape[-1] * (N_CHIPS if b3 else 1)
        tn = b.shape[-1] if b3 else None
    elif mode == "nt":
        M, K = a.shape
        N = b.shape[-2]
        tk = b.shape[-1] if b3 else None
    else:
        K, M = a.shape
        N = b.shape[1]
        tn = N // N_CHIPS if out_shards else None
    tm = tm or _pick(M, M_TILES)
    tn = tn or _pick(N, N_TILES)
    tk = tk or _pick(K, K_TILES)
    nk = K // tk
    dn = {"nn": (((1,), (0,)), ((), ())), "nt": (((1,), (1,)), ((), ())), "tn": (((0,), (0,)), ((), ()))}[mode]

    def body(a_ref, b_ref, o_ref):
        if nk == 1:
            o_ref[...] = lax.dot_general(a_ref[...], b_ref[...], dn, preferred_element_type=F32)
        else:
            @pl.when(pl.program_id(2) == 0)
            def _():
                o_ref[...] = jnp.zeros_like(o_ref)

            o_ref[...] += lax.dot_general(a_ref[...], b_ref[...], dn, preferred_element_type=F32)

    if mode == "tn":
        a_spec = pl.BlockSpec((tk, tm), lambda i, j, k: (k, i))
    else:
        a_spec = pl.BlockSpec((tm, tk), lambda i, j, k: (i, k))
    if mode == "nn":
        b_spec = pl.BlockSpec((None, tk, tn), lambda i, j, k: (j, k, 0)) if b3 else pl.BlockSpec((tk, tn), lambda i, j, k: (k, j))
    elif mode == "nt":
        b_spec = pl.BlockSpec((None, tn, tk), lambda i, j, k: (k, j, 0)) if b3 else pl.BlockSpec((tn, tk), lambda i, j, k: (j, k))
    else:
        b_spec = pl.BlockSpec((tk, tn), lambda i, j, k: (k, j))
    if out_shards:
        out_spec = pl.BlockSpec((None, tm, tn), lambda i, j, k: (j, i, 0))
        out_shape = jax.ShapeDtypeStruct((N_CHIPS, M, tn), F32)
    else:
        out_spec = pl.BlockSpec((tm, tn), lambda i, j, k: (i, j))
        out_shape = jax.ShapeDtypeStruct((M, N), F32)
    return pl.pallas_call(
        body, name=name, grid=(M // tm, N // tn, nk),
        in_specs=[a_spec, b_spec], out_specs=out_spec, out_shape=out_shape,
        compiler_params=_params(("parallel", "parallel", "arbitrary"), VMEM_LIMIT),
    )(a, b)


def _rms(x, g):
    r = lax.rsqrt(jnp.mean(x * x, axis=-1, keepdims=True) + RMS_EPS)
    return x * r * g


def _rms_bwd(x, g, dy):
    r = lax.rsqrt(jnp.mean(x * x, axis=-1, keepdims=True) + RMS_EPS)
    n = x * r
    dn = dy * g
    dx = r * (dn - n * jnp.mean(dn * n, axis=-1, keepdims=True))
    return dx, jnp.sum(dy * n, axis=0, keepdims=True)


def _sigmoid(x):
    return 1.0 / (1.0 + jnp.exp(-x))


_GELU_C = math.sqrt(2.0 / math.pi)


def _gelu(x):
    return 0.5 * x * (1.0 + jnp.tanh(_GELU_C * (x + 0.044715 * x * x * x)))


def _gelu_grad(x):
    t = jnp.tanh(_GELU_C * (x + 0.044715 * x * x * x))
    return 0.5 * (1.0 + t) + 0.5 * x * (1.0 - t * t) * _GELU_C * (1.0 + 3.0 * 0.044715 * x * x)


def _row_spec(width, col_block=0):
    return pl.BlockSpec((ROW_TILE, width), lambda i: (i, col_block))


def _vec_spec(width, col_block=0):
    return pl.BlockSpec((1, width), lambda i: (0, col_block))


def _accumulate(ref, part):
    @pl.when(pl.program_id(0) == 0)
    def _():
        ref[...] = part

    @pl.when(pl.program_id(0) > 0)
    def _():
        ref[...] += part


def _rms_fwd_call(x, g):
    S, D = x.shape

    def body(x_ref, g_ref, h_ref):
        h_ref[...] = _rms(x_ref[...], g_ref[...]).astype(BF16)

    return pl.pallas_call(
        body, name="rms_mix_pre", grid=(S // ROW_TILE,),
        in_specs=[_row_spec(D), _vec_spec(D)], out_specs=_row_spec(D),
        out_shape=jax.ShapeDtypeStruct((S, D), BF16),
        compiler_params=_params(("parallel",)),
    )(x, g)


def _ln_silu_call(c1, g, b):
    S, C = c1.shape

    def body(c_ref, g_ref, b_ref, o_ref):
        xv = c_ref[...]
        mu = jnp.mean(xv, axis=-1, keepdims=True)
        xc = xv - mu
        var = jnp.mean(xc * xc, axis=-1, keepdims=True)
        z = xc * lax.rsqrt(var + LN_EPS) * g_ref[...] + b_ref[...]
        o_ref[...] = (z * _sigmoid(z)).astype(BF16)

    return pl.pallas_call(
        body, name="conv_ln_silu", grid=(S // ROW_TILE,),
        in_specs=[_row_spec(C), _vec_spec(C), _vec_spec(C)], out_specs=_row_spec(C),
        out_shape=jax.ShapeDtypeStruct((S, C), BF16),
        compiler_params=_params(("parallel",)),
    )(c1, g, b)


def _ln_silu_bwd_call(c1, g, b, dc):
    S, C = c1.shape

    def body(c_ref, g_ref, b_ref, dc_ref, dx_ref, dg_ref, db_ref):
        xv = c_ref[...]
        mu = jnp.mean(xv, axis=-1, keepdims=True)
        xc = xv - mu
        rs = lax.rsqrt(jnp.mean(xc * xc, axis=-1, keepdims=True) + LN_EPS)
        xh = xc * rs
        z = xh * g_ref[...] + b_ref[...]
        sg = _sigmoid(z)
        dz = dc_ref[...] * (sg * (1.0 + z * (1.0 - sg)))
        dxh = dz * g_ref[...]
        dx_ref[...] = rs * (dxh - jnp.mean(dxh, axis=-1, keepdims=True) - xh * jnp.mean(dxh * xh, axis=-1, keepdims=True))
        _accumulate(dg_ref, jnp.sum(dz * xh, axis=0, keepdims=True))
        _accumulate(db_ref, jnp.sum(dz, axis=0, keepdims=True))

    return pl.pallas_call(
        body, name="conv_ln_silu_bwd", grid=(S // ROW_TILE,),
        in_specs=[_row_spec(C), _vec_spec(C), _vec_spec(C), _row_spec(C)],
        out_specs=[_row_spec(C), _vec_spec(C), _vec_spec(C)],
        out_shape=[jax.ShapeDtypeStruct((S, C), F32), jax.ShapeDtypeStruct((1, C), F32), jax.ShapeDtypeStruct((1, C), F32)],
        compiler_params=_params(("arbitrary",)),
    )(c1, g, b, dc)


def _mix_call(proj, gate_col0, b_gate, y_a, y_c):
    S, D = y_a.shape
    w = 512
    nc = D // w
    ga0, gc0 = gate_col0 // w, (gate_col0 + D) // w

    def body(ga_ref, gc_ref, ba_ref, bc_ref, ya_ref, yc_ref, o_ref):
        o_ref[...] = (_sigmoid(ga_ref[...] + ba_ref[...]) * ya_ref[...]
                      + _sigmoid(gc_ref[...] + bc_ref[...]) * yc_ref[...]).astype(BF16)

    tile = lambda off: pl.BlockSpec((ROW_TILE, w), lambda i, j: (i, off + j))
    vec = lambda off: pl.BlockSpec((1, w), lambda i, j: (0, off + j))
    return pl.pallas_call(
        body, name="gate_mix", grid=(S // ROW_TILE, nc),
        in_specs=[tile(ga0), tile(gc0), vec(0), vec(nc), tile(0), tile(0)],
        out_specs=tile(0), out_shape=jax.ShapeDtypeStruct((S, D), BF16),
        compiler_params=_params(("parallel", "parallel")),
    )(proj, proj, b_gate, b_gate, y_a, y_c)


def _mix_bwd_call(dmixed, proj, gate_col0, b_gate, y_a, y_c):
    S, D = y_a.shape
    w = 512
    nc = D // w
    ga0, gc0 = gate_col0 // w, (gate_col0 + D) // w

    def body(dm_ref, ga_ref, gc_ref, ba_ref, bc_ref, ya_ref, yc_ref, dya_ref, dyc_ref, dga_ref, dgc_ref, dba_ref, dbc_ref):
        dm = dm_ref[...]
        sa = _sigmoid(ga_ref[...] + ba_ref[...])
        sc = _sigmoid(gc_ref[...] + bc_ref[...])
        dya_ref[...] = (dm * sa).astype(BF16)
        dyc_ref[...] = (dm * sc).astype(BF16)
        dga = dm * ya_ref[...] * sa * (1.0 - sa)
        dgc = dm * yc_ref[...] * sc * (1.0 - sc)
        dga_ref[...] = dga.astype(BF16)
        dgc_ref[...] = dgc.astype(BF16)
        pa = jnp.sum(dga, axis=0, keepdims=True)
        pc = jnp.sum(dgc, axis=0, keepdims=True)

        @pl.when(pl.program_id(1) == 0)
        def _():
            dba_ref[...] = pa
            dbc_ref[...] = pc

        @pl.when(pl.program_id(1) > 0)
        def _():
            dba_ref[...] += pa
            dbc_ref[...] += pc

    tile = lambda off: pl.BlockSpec((ROW_TILE, w), lambda j, i: (i, off + j))
    vec = lambda off: pl.BlockSpec((1, w), lambda j, i: (0, off + j))
    return pl.pallas_call(
        body, name="gate_mix_bwd", grid=(nc, S // ROW_TILE),
        in_specs=[tile(0), tile(ga0), tile(gc0), vec(0), vec(nc), tile(0), tile(0)],
        out_specs=[tile(0), tile(0), tile(0), tile(0), vec(0), vec(0)],
        out_shape=[jax.ShapeDtypeStruct((S, D), BF16)] * 4 + [
                   jax.ShapeDtypeStruct((1, D), F32), jax.ShapeDtypeStruct((1, D), F32)],
        compiler_params=_params(("parallel", "arbitrary")),
    )(dmixed, proj, proj, b_gate, b_gate, y_a, y_c)


def _res1_call(x, out, g_post, g_pre):
    S, D = x.shape

    def body(x_ref, o_ref, gp_ref, gq_ref, x1_ref, h2_ref):
        x1 = x_ref[...] + _rms(o_ref[...], gp_ref[...])
        x1_ref[...] = x1
        h2_ref[...] = _rms(x1, gq_ref[...]).astype(BF16)

    return pl.pallas_call(
        body, name="residual_mix", grid=(S // ROW_TILE,),
        in_specs=[_row_spec(D), _row_spec(D), _vec_spec(D), _vec_spec(D)],
        out_specs=[_row_spec(D), _row_spec(D)],
        out_shape=[jax.ShapeDtypeStruct((S, D), F32), jax.ShapeDtypeStruct((S, D), BF16)],
        compiler_params=_params(("parallel",)),
    )(x, out, g_post, g_pre)


def _loss_call(y, x1, g_post, target):
    S, D = y.shape

    def body(y_ref, x1_ref, g_ref, t_ref, loss_ref, dx_ref):
        err = x1_ref[...] + _rms(y_ref[...], g_ref[...]) - t_ref[...]
        dx_ref[...] = err * (1.0 / D)
        part = 0.5 * jnp.sum(jnp.mean(err * err, axis=-1, keepdims=True), axis=0, keepdims=True)
        _accumulate(loss_ref, jnp.broadcast_to(part, (SUBLANES, LANES)))

    return pl.pallas_call(
        body, name="residual_ffn_loss", grid=(S // ROW_TILE,),
        in_specs=[_row_spec(D), _row_spec(D), _vec_spec(D), _row_spec(D)],
        out_specs=[pl.BlockSpec((SUBLANES, LANES), lambda i: (0, 0)), _row_spec(D)],
        out_shape=[jax.ShapeDtypeStruct((SUBLANES, LANES), F32), jax.ShapeDtypeStruct((S, D), F32)],
        compiler_params=_params(("arbitrary",)),
    )(y, x1, g_post, target)


def _rms_bwd_call(x, g, dy, name):
    S, D = x.shape

    def body(x_ref, g_ref, dy_ref, dx_ref, dg_ref):
        dx, dg = _rms_bwd(x_ref[...], g_ref[...], dy_ref[...])
        dx_ref[...] = dx.astype(BF16)
        _accumulate(dg_ref, dg)

    return pl.pallas_call(
        body, name=name, grid=(S // ROW_TILE,),
        in_specs=[_row_spec(D), _vec_spec(D), _row_spec(D)],
        out_specs=[_row_spec(D), _vec_spec(D)],
        out_shape=[jax.ShapeDtypeStruct((S, D), BF16), jax.ShapeDtypeStruct((1, D), F32)],
        compiler_params=_params(("arbitrary",)),
    )(x, g, dy)


def _mid_bwd_call(x1, g_pre, dh2, dx2, out, g_post):
    S, D = x1.shape

    def body(x1_ref, gq_ref, dh_ref, dx2_ref, o_ref, gp_ref, dx1_ref, do_ref, dgq_ref, dgp_ref):
        d, dgq = _rms_bwd(x1_ref[...], gq_ref[...], dh_ref[...])
        dx1 = dx2_ref[...] + d
        dx1_ref[...] = dx1
        do, dgp = _rms_bwd(o_ref[...], gp_ref[...], dx1)
        do_ref[...] = do.astype(BF16)
        _accumulate(dgq_ref, dgq)
        _accumulate(dgp_ref, dgp)

    return pl.pallas_call(
        body, name="residual_mix_bwd", grid=(S // ROW_TILE,),
        in_specs=[_row_spec(D), _vec_spec(D), _row_spec(D), _row_spec(D), _row_spec(D), _vec_spec(D)],
        out_specs=[_row_spec(D), _row_spec(D), _vec_spec(D), _vec_spec(D)],
        out_shape=[jax.ShapeDtypeStruct((S, D), F32), jax.ShapeDtypeStruct((S, D), BF16)] + [jax.ShapeDtypeStruct((1, D), F32)] * 2,
        compiler_params=_params(("arbitrary",)),
    )(x1, g_pre, dh2, dx2, out, g_post)


def _in_bwd_call(x, g, dh1, dx1):
    S, D = x.shape

    def body(x_ref, g_ref, dh_ref, dx1_ref, gx_ref, dg_ref):
        d, dg = _rms_bwd(x_ref[...], g_ref[...], dh_ref[...])
        gx_ref[...] = dx1_ref[...] + d
        _accumulate(dg_ref, dg)

    return pl.pallas_call(
        body, name="rms_mix_pre_bwd", grid=(S // ROW_TILE,),
        in_specs=[_row_spec(D), _vec_spec(D), _row_spec(D), _row_spec(D)],
        out_specs=[_row_spec(D), _vec_spec(D)],
        out_shape=[jax.ShapeDtypeStruct((S, D), F32), jax.ShapeDtypeStruct((1, D), F32)],
        compiler_params=_params(("arbitrary",)),
    )(x, g, dh1, dx1)


def _bucket_table(dilation):
    qi = np.arange(SPAN)[:, None]
    ki = np.arange(2 * SPAN)[None, :]
    dist = np.maximum(qi + SPAN - ki, 0) * dilation
    max_exact = N_BUCKETS // 2
    d = np.maximum(dist, 1).astype(np.float64)
    large = max_exact + (np.log(d / max_exact) / math.log(MAX_DISTANCE / max_exact) * (N_BUCKETS - max_exact)).astype(np.int32)
    large = np.minimum(large, N_BUCKETS - 1)
    return np.where(dist < max_exact, dist, large).astype(np.int32)


def _bucket_tables():
    return jnp.asarray(np.stack([_bucket_table(r) for _, r in DILATED_PATTERNS]))


def _bias_table_call(rel_bias, buckets):
    def body(rb_ref, bk_ref, o_ref):
        for h in range(N_HEADS):
            bk = bk_ref[h // HEADS_PER_GROUP]

            def step(b, acc):
                return jnp.where(bk == b, rb_ref[b, h], acc)

            o_ref[h] = lax.fori_loop(0, N_BUCKETS, step, jnp.zeros((SPAN, 2 * SPAN), F32))

    return pl.pallas_call(
        body, name="rel_bias_table",
        in_specs=[pl.BlockSpec(memory_space=pltpu.SMEM), pl.BlockSpec(memory_space=pltpu.VMEM)],
        out_specs=pl.BlockSpec(memory_space=pltpu.VMEM),
        out_shape=jax.ShapeDtypeStruct((N_HEADS, SPAN, 2 * SPAN), F32),
    )(rel_bias, buckets)


def _bias_grad_call(dbias, buckets):
    def body(db_ref, bk_ref, o_ref, rows_ref):
        for h in range(N_HEADS):
            bk = bk_ref[h // HEADS_PER_GROUP]
            dv = db_ref[h]

            def step(b, carry):
                rows_ref[h, b] = jnp.sum(jnp.where(bk == b, dv, 0.0), axis=0, keepdims=True)
                return carry

            lax.fori_loop(0, N_BUCKETS, step, 0)
        o_ref[...] = jnp.sum(rows_ref[...], axis=-1, keepdims=True)

    out = pl.pallas_call(
        body, name="rel_bias_grad",
        in_specs=[pl.BlockSpec(memory_space=pltpu.VMEM), pl.BlockSpec(memory_space=pltpu.VMEM)],
        out_specs=pl.BlockSpec(memory_space=pltpu.VMEM),
        out_shape=jax.ShapeDtypeStruct((N_HEADS, N_BUCKETS, 1, 1), F32),
        scratch_shapes=[pltpu.VMEM((N_HEADS, N_BUCKETS, 1, 2 * SPAN), F32)],
    )(dbias, buckets)
    return out.reshape(N_HEADS, N_BUCKETS).T


def _dot_nt(a, b):
    return lax.dot_general(a, b, (((1,), (1,)), ((), ())), preferred_element_type=F32)


def _dot_nn(a, b):
    return lax.dot_general(a, b, (((1,), (0,)), ((), ())), preferred_element_type=F32)


def _dot_tn(a, b):
    return lax.dot_general(a, b, (((0,), (0,)), ((), ())), preferred_element_type=F32)


def _band_masks(n, nb):
    qi = lax.broadcasted_iota(jnp.int32, (SPAN, SPAN), 0)
    ki = lax.broadcasted_iota(jnp.int32, (SPAN, SPAN), 1)
    prev_ok = jnp.logical_and(ki >= qi, n > 0)
    cur_ok = ki <= qi
    next_ok = jnp.logical_and(ki >= qi, n < nb - 1)
    return prev_ok, cur_ok, next_ok


def _wide_band_mask(n):
    qi = lax.broadcasted_iota(jnp.int32, (SPAN, 2 * SPAN), 0)
    ki = lax.broadcasted_iota(jnp.int32, (SPAN, 2 * SPAN), 1)
    prev_ok = jnp.logical_and(jnp.logical_and(ki < SPAN, ki >= qi), n > 0)
    cur_ok = jnp.logical_and(ki >= SPAN, ki - SPAN <= qi)
    return jnp.logical_or(prev_ok, cur_ok)


def _attn_plan(S, group):
    r = DILATED_PATTERNS[group][1]
    hp, per = (HEADS_PER_GROUP, 1) if r == 1 else (2, 4)
    return r, S // (r * SPAN), hp, per


def _residue_rows(rho, r):
    return slice(None) if r == 1 else pl.ds(rho, SPAN, stride=r)


def _for_residues(r, per, fn):
    if r == per:
        for u in range(per):
            fn(u)
        return

    def step(i, carry):
        for u in range(per):
            fn(i * per + u)
        return carry

    lax.fori_loop(0, r // per, step, 0)


def _attn_fwd_call(proj, bias, group):
    S = proj.shape[0]
    r, nb, hp, per = _attn_plan(S, group)
    scale = HEAD_DIM ** -0.5
    kinds = ("q", "kp", "kc", "vp", "vc") if nb > 1 else ("q", "kc", "vc")

    def body(*refs):
        ins = {kind: refs[i * hp:(i + 1) * hp] for i, kind in enumerate(kinds)}
        b_ref, o_ref, lse_ref = refs[len(kinds) * hp:]
        n = pl.program_id(1)
        prev_ok, cur_ok, _ = _band_masks(n, nb)

        band_ok = _wide_band_mask(n) if nb > 1 else cur_ok

        def residue(rho):
            rows = _residue_rows(rho, r)
            for j in range(hp):
                get = lambda kind: ins[kind][j][rows, :].astype(BF16)
                q = get("q")
                if nb > 1:
                    keys, vals, bias_j = jnp.concatenate([get("kp"), get("kc")], axis=0), jnp.concatenate([get("vp"), get("vc")], axis=0), b_ref[j]
                else:
                    keys, vals, bias_j = get("kc"), get("vc"), b_ref[j, :, SPAN:]
                s = jnp.where(band_ok, _dot_nt(q, keys) * scale + bias_j, NEG_INF)
                m = jnp.max(s, axis=-1, keepdims=True)
                p = jnp.exp(s - m)
                den = jnp.sum(p, axis=-1, keepdims=True)
                o_ref[j, rows, :] = _dot_nn(p.astype(BF16), vals) / den
                lse_ref[j, rows, :] = jnp.broadcast_to(m + jnp.log(den), (SPAN, HEAD_DIM))

        _for_residues(r, per, residue)

    in_specs = [_head_spec(r, nb, hp, kind, group, jj) for kind in kinds for jj in range(hp)]
    in_specs.append(pl.BlockSpec((hp, SPAN, 2 * SPAN), lambda j, n: (group * (HEADS_PER_GROUP // hp) + j, 0, 0)))
    out = pl.BlockSpec((hp, r * SPAN, HEAD_DIM), lambda j, n: (j, n, 0))
    return pl.pallas_call(
        body, name=f"attn_fwd_g{group}", grid=(HEADS_PER_GROUP // hp, nb),
        in_specs=in_specs, out_specs=[out] * 2,
        out_shape=[jax.ShapeDtypeStruct((HEADS_PER_GROUP, S, HEAD_DIM), F32)] * 2,
        compiler_params=_params(("parallel", "parallel"), VMEM_LIMIT),
    )(*([proj] * (len(in_specs) - 1)), bias)


_PROJ_PART = dict(q=0, qn=0, kp=1, kc=1, vp=2, vc=2)


def _head_spec(r, nb, hp, kind, group, jj):
    if kind in _PROJ_PART:
        base = (_PROJ_PART[kind] * N_GROUPS + group) * HEADS_PER_GROUP
    else:
        base = 0
    if kind.endswith("p"):
        row = lambda n: jnp.maximum(n - 1, 0)
    elif kind.endswith("n"):
        row = lambda n: jnp.minimum(n + 1, nb - 1)
    else:
        row = lambda n: n
    return pl.BlockSpec((r * SPAN, HEAD_DIM), lambda j, n: (row(n), base + j * hp + jj))


def _attn_merge_call(parts):
    S = parts[0].shape[1]

    def body(o1, s1, o2, s2, o3, s3, a_ref, ab_ref, lse_ref):
        for j in range(HEADS_PER_GROUP):
            sl = slice(j * HEAD_DIM, (j + 1) * HEAD_DIM)
            mx = jnp.maximum(jnp.maximum(s1[j], s2[j]), s3[j])
            w1 = jnp.exp(s1[j] - mx)
            w2 = jnp.exp(s2[j] - mx)
            w3 = jnp.exp(s3[j] - mx)
            den = w1 + w2 + w3
            a = (w1 * o1[j] + w2 * o2[j] + w3 * o3[j]) / den
            a_ref[:, sl] = a
            ab_ref[:, sl] = a.astype(BF16)
            lse_ref[:, sl] = mx + jnp.log(den)

    heads = pl.BlockSpec((HEADS_PER_GROUP, ROW_TILE, HEAD_DIM), lambda i: (0, i, 0))
    return pl.pallas_call(
        body, name="attn_merge", grid=(S // ROW_TILE,),
        in_specs=[heads] * 6, out_specs=[_row_spec(GROUP_WIDTH)] * 3,
        out_shape=[jax.ShapeDtypeStruct((S, GROUP_WIDTH), F32), jax.ShapeDtypeStruct((S, GROUP_WIDTH), BF16),
                   jax.ShapeDtypeStruct((S, GROUP_WIDTH), F32)],
        compiler_params=_params(("parallel",)),
    )(*parts)


def _attn_delta_call(a, da):
    S = a.shape[0]

    def body(a_ref, da_ref, d_ref):
        for j in range(HEADS_PER_GROUP):
            sl = slice(j * HEAD_DIM, (j + 1) * HEAD_DIM)
            d = jnp.sum(a_ref[:, sl] * da_ref[:, sl], axis=-1, keepdims=True)
            d_ref[:, sl] = jnp.broadcast_to(d, (ROW_TILE, HEAD_DIM))

    return pl.pallas_call(
        body, name="attn_delta", grid=(S // ROW_TILE,),
        in_specs=[_row_spec(GROUP_WIDTH)] * 2, out_specs=_row_spec(GROUP_WIDTH),
        out_shape=jax.ShapeDtypeStruct((S, GROUP_WIDTH), F32),
        compiler_params=_params(("parallel",)),
    )(a, da)


def _attn_bwd_call(proj, bias, da, lse, delta, group):
    S = proj.shape[0]
    r, nb, hp, per = _attn_plan(S, group)
    scale = HEAD_DIM ** -0.5
    kinds = ("q", "qn", "kp", "kc", "vp", "vc", "da", "dan", "lse", "lsen", "dl", "dln") if nb > 1 else ("q", "kc", "vc", "da", "lse", "dl")
    source = dict(da=da, dan=da, lse=lse, lsen=lse, dl=delta, dln=delta)

    def body(*refs):
        ins = {kind: refs[i * hp:(i + 1) * hp] for i, kind in enumerate(kinds)}
        b_ref, dq_ref, dk_ref, dv_ref, db_ref = refs[len(kinds) * hp:]
        n = pl.program_id(1)
        prev_ok, cur_ok, next_ok = _band_masks(n, nb)

        @pl.when(n == 0)
        def _():
            db_ref[...] = jnp.zeros_like(db_ref)

        band_ok = _wide_band_mask(n) if nb > 1 else cur_ok

        def residue(rho):
            rows = _residue_rows(rho, r)
            for j in range(hp):
                get = lambda kind: ins[kind][j][rows, :]
                q = get("q").astype(BF16)
                kc = get("kc").astype(BF16)
                vc = get("vc").astype(BF16)
                dav = get("da").astype(BF16)
                lse_q, dl_q = get("lse"), get("dl")
                if nb == 1:
                    pc = jnp.exp(jnp.where(cur_ok, _dot_nt(q, kc) * scale + b_ref[j, :, SPAN:], NEG_INF) - lse_q)
                    dsc = pc * (_dot_nt(dav, vc) - dl_q)
                    dsc_b = dsc.astype(BF16)
                    dq = _dot_nn(dsc_b, kc)
                    dk = _dot_tn(dsc_b, q)
                    dv = _dot_tn(pc.astype(BF16), dav)
                    db_ref[j, :, SPAN:] += dsc
                else:
                    qn = get("qn").astype(BF16)
                    dan = get("dan").astype(BF16)
                    keys = jnp.concatenate([get("kp").astype(BF16), kc], axis=0)
                    vals = jnp.concatenate([get("vp").astype(BF16), vc], axis=0)
                    wide = lambda t: jnp.concatenate([t, t], axis=1)
                    p = jnp.exp(jnp.where(band_ok, _dot_nt(q, keys) * scale + b_ref[j], NEG_INF) - wide(lse_q))
                    ds = p * (_dot_nt(dav, vals) - wide(dl_q))
                    dq = _dot_nn(ds.astype(BF16), keys)
                    db_ref[j] += ds
                    pn = jnp.exp(jnp.where(next_ok, _dot_nt(qn, kc) * scale + b_ref[j, :, :SPAN], NEG_INF) - get("lsen"))
                    dsn = pn * (_dot_nt(dan, vc) - get("dln"))
                    both = lambda cur_part, next_part: jnp.concatenate([cur_part.astype(BF16), next_part.astype(BF16)], axis=0)
                    dk = _dot_tn(both(ds[:, SPAN:], dsn), jnp.concatenate([q, qn], axis=0))
                    dv = _dot_tn(both(p[:, SPAN:], pn), jnp.concatenate([dav, dan], axis=0))
                dq_ref[j, rows, :] = dq * scale
                dk_ref[j, rows, :] = dk * scale
                dv_ref[j, rows, :] = dv

        _for_residues(r, per, residue)

    per_group = HEADS_PER_GROUP // hp
    band = (hp, SPAN, 2 * SPAN)
    in_specs = [_head_spec(r, nb, hp, kind, group, jj) for kind in kinds for jj in range(hp)]
    in_specs.append(pl.BlockSpec(band, lambda j, n: (group * per_group + j, 0, 0)))
    operands = [source.get(kind, proj) for kind in kinds for _ in range(hp)] + [bias]
    out = pl.BlockSpec((hp, r * SPAN, HEAD_DIM), lambda j, n: (j, n, 0))
    return pl.pallas_call(
        body, name=f"attn_bwd_g{group}", grid=(per_group, nb),
        in_specs=in_specs,
        out_specs=[out] * 3 + [pl.BlockSpec(band, lambda j, n: (j, 0, 0))],
        out_shape=[jax.ShapeDtypeStruct((HEADS_PER_GROUP, S, HEAD_DIM), F32)] * 3
        + [jax.ShapeDtypeStruct((HEADS_PER_GROUP, SPAN, 2 * SPAN), F32)],
        compiler_params=_params(("parallel", "arbitrary"), VMEM_LIMIT),
    )(*operands)


def _dproj_call(dqkv, tails):
    S = tails[0].shape[0]
    width = len(dqkv) * GROUP_WIDTH + sum(t.shape[1] for t in tails)

    def body(*refs):
        o_ref = refs[-1]
        col = 0
        for ref in refs[:len(dqkv)]:
            for j in range(HEADS_PER_GROUP):
                o_ref[:, col:col + HEAD_DIM] = ref[j].astype(BF16)
                col += HEAD_DIM
        for ref in refs[len(dqkv):-1]:
            o_ref[:, col:col + ref.shape[1]] = ref[...]
            col += ref.shape[1]

    heads = pl.BlockSpec((HEADS_PER_GROUP, ROW_TILE, HEAD_DIM), lambda i: (0, i, 0))
    return pl.pallas_call(
        body, name="dproj_assemble", grid=(S // ROW_TILE,),
        in_specs=[heads] * len(dqkv) + [_row_spec(t.shape[1]) for t in tails],
        out_specs=_row_spec(width), out_shape=jax.ShapeDtypeStruct((S, width), BF16),
        compiler_params=_params(("parallel",)),
    )(*dqkv, *tails)


def _taps(width):
    return [(k, (width - 1 - k) // SUBLANES, (width - 1 - k) % SUBLANES) for k in range(width)]


def _shifted(win, width, pad, up):
    total = win.shape[0]
    for b in range(SUBLANES):
        taps = [(k, a) for k, a, bb in _taps(width) if bb == b]
        if not taps:
            continue
        if up:
            rolled = win if b == 0 else pltpu.roll(win, total - b, axis=0)
        else:
            rolled = win if b == 0 else pltpu.roll(win, b, axis=0)
        for k, a in taps:
            start = SUBLANES * a if up else pad - SUBLANES * a
            yield k, rolled[start:start + TIME_BLOCK, :]


def _conv_block(win, w_ref, width, pad):
    acc = None
    for k, rows in _shifted(win, width, pad, up=False):
        term = w_ref[k:k + 1, :] * rows
        acc = term if acc is None else acc + term
    return acc


def _conv_transpose_block(win, w_ref, width, pad):
    acc = None
    for k, rows in _shifted(win, width, pad, up=True):
        term = w_ref[k:k + 1, :] * rows
        acc = term if acc is None else acc + term
    return acc


def _conv_weight_grad(win, dy, dw_ref, width, pad):
    for k, rows in _shifted(win, width, pad, up=False):
        dw_ref[k:k + 1, :] += jnp.sum(dy * rows, axis=0, keepdims=True)


def _time_loop(S, step):
    def it(tb, carry):
        step(pl.multiple_of(tb * TIME_BLOCK, TIME_BLOCK))
        return carry

    lax.fori_loop(0, S // TIME_BLOCK, it, 0)


def _conv_fwd_call(proj, col0, w, b):
    S = proj.shape[0]
    C = w.shape[1]
    nt = C // LANES
    v0, g0 = col0 // LANES, (col0 + C) // LANES

    def body(val_ref, gate_ref, w_ref, b_ref, o_ref, pad_ref):
        pad_ref[0:CONV_PAD, :] = jnp.zeros((CONV_PAD, LANES), F32)
        pad_ref[CONV_PAD:, :] = val_ref[...] * _sigmoid(gate_ref[...])

        def step(t0):
            win = pad_ref[pl.ds(t0, TIME_BLOCK + CONV_PAD), :]
            o_ref[pl.ds(t0, TIME_BLOCK), :] = _conv_block(win, w_ref, CONV_WIDTH, CONV_PAD) + b_ref[...]

        _time_loop(S, step)

    seq = lambda off: pl.BlockSpec((S, LANES), lambda i: (0, off + i))
    return pl.pallas_call(
        body, name="conv_module", grid=(nt,),
        in_specs=[seq(v0), seq(g0), pl.BlockSpec((CONV_WIDTH, LANES), lambda i: (0, i)), pl.BlockSpec((1, LANES), lambda i: (0, i))],
        out_specs=seq(0), out_shape=jax.ShapeDtypeStruct((S, C), F32),
        scratch_shapes=[pltpu.VMEM((S + CONV_PAD, LANES), F32)],
        compiler_params=_params(("parallel",)),
    )(proj, proj, w, b)


def _conv_bwd_call(proj, col0, w, dc1):
    S = proj.shape[0]
    C = w.shape[1]
    nt = C // LANES
    v0, g0 = col0 // LANES, (col0 + C) // LANES

    def body(val_ref, gate_ref, w_ref, dy_ref, dval_ref, dgate_ref, dw_ref, db_ref, xpad_ref, dpad_ref, dwacc_ref):
        xpad_ref[0:CONV_PAD, :] = jnp.zeros((CONV_PAD, LANES), F32)
        xpad_ref[CONV_PAD:, :] = val_ref[...] * _sigmoid(gate_ref[...])
        dpad_ref[0:S, :] = dy_ref[...]
        dpad_ref[S:, :] = jnp.zeros((CONV_PAD, LANES), F32)
        dwacc_ref[...] = jnp.zeros_like(dwacc_ref)

        def step(t0):
            rows = pl.ds(t0, TIME_BLOCK)
            _conv_weight_grad(xpad_ref[pl.ds(t0, TIME_BLOCK + CONV_PAD), :], dy_ref[rows, :], dwacc_ref, CONV_WIDTH, CONV_PAD)
            dc0 = _conv_transpose_block(dpad_ref[pl.ds(t0, TIME_BLOCK + CONV_PAD), :], w_ref, CONV_WIDTH, CONV_PAD)
            sg = _sigmoid(gate_ref[rows, :])
            dval_ref[rows, :] = (dc0 * sg).astype(BF16)
            dgate_ref[rows, :] = (dc0 * val_ref[rows, :] * sg * (1.0 - sg)).astype(BF16)

        _time_loop(S, step)
        dw_ref[...] = dwacc_ref[...]
        db_ref[...] = jnp.sum(dy_ref[...], axis=0, keepdims=True)

    seq = lambda off: pl.BlockSpec((S, LANES), lambda i: (0, off + i))
    return pl.pallas_call(
        body, name="conv_module_bwd", grid=(nt,),
        in_specs=[seq(v0), seq(g0), pl.BlockSpec((CONV_WIDTH, LANES), lambda i: (0, i)), seq(0)],
        out_specs=[seq(0), seq(0), pl.BlockSpec((CONV_PAD, LANES), lambda i: (0, i)), pl.BlockSpec((1, LANES), lambda i: (0, i))],
        out_shape=[jax.ShapeDtypeStruct((S, C), BF16), jax.ShapeDtypeStruct((S, C), BF16),
                   jax.ShapeDtypeStruct((CONV_PAD, C), F32), jax.ShapeDtypeStruct((1, C), F32)],
        scratch_shapes=[pltpu.VMEM((S + CONV_PAD, LANES), F32), pltpu.VMEM((S + CONV_PAD, LANES), F32),
                        pltpu.VMEM((CONV_PAD, LANES), F32)],
        compiler_params=_params(("parallel",)),
    )(proj, proj, w, dc1)


def _ffn_fwd_call(u, w, b):
    S, C2 = u.shape
    C = C2 // 2
    nt = C // LANES

    def body(ug_ref, uv_ref, wg_ref, wv_ref, bg_ref, bv_ref, f_ref, pg_ref, pv_ref):
        zeros = jnp.zeros((FFN_PAD, LANES), F32)
        pg_ref[0:FFN_PAD, :] = zeros
        pv_ref[0:FFN_PAD, :] = zeros
        pg_ref[FFN_PAD:, :] = ug_ref[...]
        pv_ref[FFN_PAD:, :] = uv_ref[...]

        def step(t0):
            win = pl.ds(t0, TIME_BLOCK + FFN_PAD)
            cg = _conv_block(pg_ref[win, :], wg_ref, FFN_CONV_WIDTH, FFN_PAD) + bg_ref[...]
            cv = _conv_block(pv_ref[win, :], wv_ref, FFN_CONV_WIDTH, FFN_PAD) + bv_ref[...]
            f_ref[pl.ds(t0, TIME_BLOCK), :] = (_gelu(cg) * cv).astype(BF16)

        _time_loop(S, step)

    seq = lambda off: pl.BlockSpec((S, LANES), lambda i: (0, off + i))
    wsp = lambda off: pl.BlockSpec((FFN_CONV_WIDTH, LANES), lambda i: (0, off + i))
    bsp = lambda off: pl.BlockSpec((1, LANES), lambda i: (0, off + i))
    return pl.pallas_call(
        body, name="ffn_conv_geglu", grid=(nt,),
        in_specs=[seq(0), seq(nt), wsp(0), wsp(nt), bsp(0), bsp(nt)],
        out_specs=seq(0), out_shape=jax.ShapeDtypeStruct((S, C), BF16),
        scratch_shapes=[pltpu.VMEM((S + FFN_PAD, LANES), F32)] * 2,
        compiler_params=_params(("parallel",)),
    )(u, u, w, w, b, b)


def _ffn_bwd_call(u, w, b, df):
    S, C2 = u.shape
    C = C2 // 2
    nt = C // LANES

    def body(ug_ref, uv_ref, wg_ref, wv_ref, bg_ref, bv_ref, df_ref,
             dug_ref, duv_ref, dwg_ref, dwv_ref, dbg_ref, dbv_ref,
             pg_ref, pv_ref, dg_ref, dv_ref, dwg_acc, dwv_acc, dbg_acc, dbv_acc):
        zeros = jnp.zeros((FFN_PAD, LANES), F32)
        pg_ref[0:FFN_PAD, :] = zeros
        pv_ref[0:FFN_PAD, :] = zeros
        pg_ref[FFN_PAD:, :] = ug_ref[...]
        pv_ref[FFN_PAD:, :] = uv_ref[...]
        dg_ref[S:, :] = zeros
        dv_ref[S:, :] = zeros
        dwg_acc[...] = jnp.zeros_like(dwg_acc)
        dwv_acc[...] = jnp.zeros_like(dwv_acc)
        dbg_acc[...] = jnp.zeros_like(dbg_acc)
        dbv_acc[...] = jnp.zeros_like(dbv_acc)

        def first(t0):
            win = pl.ds(t0, TIME_BLOCK + FFN_PAD)
            rows = pl.ds(t0, TIME_BLOCK)
            xg = pg_ref[win, :]
            xv = pv_ref[win, :]
            cg = _conv_block(xg, wg_ref, FFN_CONV_WIDTH, FFN_PAD) + bg_ref[...]
            cv = _conv_block(xv, wv_ref, FFN_CONV_WIDTH, FFN_PAD) + bv_ref[...]
            dfb = df_ref[rows, :]
            dcg = dfb * cv * _gelu_grad(cg)
            dcv = dfb * _gelu(cg)
            dg_ref[rows, :] = dcg
            dv_ref[rows, :] = dcv
            _conv_weight_grad(xg, dcg, dwg_acc, FFN_CONV_WIDTH, FFN_PAD)
            _conv_weight_grad(xv, dcv, dwv_acc, FFN_CONV_WIDTH, FFN_PAD)
            dbg_acc[...] += jnp.sum(dcg, axis=0, keepdims=True)
            dbv_acc[...] += jnp.sum(dcv, axis=0, keepdims=True)

        def second(t0):
            win = pl.ds(t0, TIME_BLOCK + FFN_PAD)
            rows = pl.ds(t0, TIME_BLOCK)
            dug_ref[rows, :] = _conv_transpose_block(dg_ref[win, :], wg_ref, FFN_CONV_WIDTH, FFN_PAD).astype(BF16)
            duv_ref[rows, :] = _conv_transpose_block(dv_ref[win, :], wv_ref, FFN_CONV_WIDTH, FFN_PAD).astype(BF16)

        _time_loop(S, first)
        _time_loop(S, second)
        dwg_ref[...] = dwg_acc[...]
        dwv_ref[...] = dwv_acc[...]
        dbg_ref[...] = dbg_acc[...]
        dbv_ref[...] = dbv_acc[...]

    seq = lambda off: pl.BlockSpec((S, LANES), lambda i: (0, off + i))
    wsp = lambda off: pl.BlockSpec((FFN_CONV_WIDTH, LANES), lambda i: (0, off + i))
    bsp = lambda off: pl.BlockSpec((1, LANES), lambda i: (0, off + i))
    return pl.pallas_call(
        body, name="ffn_conv_geglu_bwd", grid=(nt,),
        in_specs=[seq(0), seq(nt), wsp(0), wsp(nt), bsp(0), bsp(nt), seq(0)],
        out_specs=[seq(0), seq(0), pl.BlockSpec((SUBLANES, LANES), lambda i: (0, i)), pl.BlockSpec((SUBLANES, LANES), lambda i: (0, i)),
                   bsp(0), bsp(0)],
        out_shape=[jax.ShapeDtypeStruct((S, C), BF16)] * 2 + [jax.ShapeDtypeStruct((SUBLANES, C), F32)] * 2
        + [jax.ShapeDtypeStruct((1, C), F32)] * 2,
        scratch_shapes=[pltpu.VMEM((S + FFN_PAD, LANES), F32)] * 4 + [pltpu.VMEM((SUBLANES, LANES), F32)] * 2
        + [pltpu.VMEM((1, LANES), F32)] * 2,
        compiler_params=_params(("parallel",)),
    )(u, u, w, w, b, b, df)


def _adamw(w_ref, g_ref, m_ref, v_ref, d_ref, mo_ref, vo_ref):
    gv = g_ref[...]
    mn = ADAM_B1 * m_ref[...] + (1.0 - ADAM_B1) * gv
    vn = ADAM_B2 * v_ref[...] + (1.0 - ADAM_B2) * (gv * gv)
    mo_ref[...] = mn
    vo_ref[...] = vn
    m_hat = mn * (1.0 / (1.0 - ADAM_B1 ** ADAM_STEP))
    v_hat = vn * (1.0 / (1.0 - ADAM_B2 ** ADAM_STEP))
    d_ref[...] = -ADAM_LR * (m_hat / (jnp.sqrt(v_hat) + ADAM_EPS) + ADAM_WD * w_ref[...])


def _adamw_call(w, g, m, v, name):
    R, C = w.shape
    tr = _row_tile(R, C)
    spec = pl.BlockSpec((tr, C), lambda i: (i, 0))
    return pl.pallas_call(
        _adamw_body(), name=name, grid=(R // tr,),
        in_specs=[spec] * 4, out_specs=[spec] * 3,
        out_shape=[jax.ShapeDtypeStruct((R, C), F32)] * 3,
        compiler_params=_params(("parallel",)),
    )(w, g, m, v)


def _adamw_body():
    def body(*refs):
        _adamw(*refs)

    return body


def _adamw_small_call(ws, gs, ms, vs):
    n = len(ws)

    def body(*refs):
        w_refs, g_refs, m_refs, v_refs, d_refs, mo_refs, vo_refs = (refs[i * n:(i + 1) * n] for i in range(7))
        for i in range(n):
            _adamw(w_refs[i], g_refs[i], m_refs[i], v_refs[i], d_refs[i], mo_refs[i], vo_refs[i])

    whole = pl.BlockSpec(memory_space=pltpu.VMEM)
    outs = pl.pallas_call(
        body, name="adamw_small",
        in_specs=[whole] * (4 * n), out_specs=[whole] * (3 * n),
        out_shape=[jax.ShapeDtypeStruct(w.shape, F32) for w in ws] * 3,
    )(*ws, *gs, *ms, *vs)
    return outs[:n], outs[n:2 * n], outs[2 * n:]


def _position():
    return lax.axis_index("x"), lax.axis_index("y"), lax.axis_index("c")


def _chip_peers(x, y):
    return [(x, 1 - y), (1 - x, y), (1 - x, 1 - y)]


def _half_rows(ref, core, rows):
    h = rows // 2
    start = pl.multiple_of(core * h, 16)
    return ref.at[pl.ds(start, h), :] if len(ref.shape) == 2 else ref.at[:, pl.ds(start, h), :]


def _shard_half(ref, shard, core, rows):
    h = rows // 2
    return ref.at[shard, pl.ds(pl.multiple_of(core * h, 16), h), :]


ANY = pl.BlockSpec(memory_space=pl.ANY)


def _allgather_call(shards, whole):
    n, nw = len(shards), len(whole)
    outs_shape = [jax.ShapeDtypeStruct((N_CHIPS,) + s.shape, s.dtype) for s in shards + whole]

    def body(*refs):
        ins, outs = refs[:n + nw], refs[n + nw:2 * (n + nw)]
        send_sems, recv_sems, pass_send, pass_recv, own_send, own_recv = refs[2 * (n + nw):]
        x, y, c = _position()
        chip = 2 * x + y
        peers = _chip_peers(x, y)
        sent, local = [], []
        for i in range(n + nw):
            cp = pltpu.make_async_remote_copy(src_ref=ins[i], dst_ref=outs[i].at[chip], send_sem=own_send.at[i],
                                              recv_sem=own_recv.at[i], device_id=(x, y, 1 - c), device_id_type=MESH)
            cp.start()
            local.append(cp)
            rows = ins[i].shape[0]
            for k, (px, py) in enumerate(peers):
                if i < n:
                    src, dst = _half_rows(ins[i], c, rows), _shard_half(outs[i], chip, c, rows)
                else:
                    src, dst = ins[i], outs[i].at[chip]
                cp = pltpu.make_async_remote_copy(src_ref=src, dst_ref=dst, send_sem=send_sems.at[i, k],
                                                  recv_sem=recv_sems.at[i, k], device_id=(px, py, c), device_id_type=MESH)
                cp.start()
                sent.append(cp)
        passed = []
        for i in range(n + nw):
            rows = ins[i].shape[0]
            for k, (px, py) in enumerate(peers):
                landed = _shard_half(outs[i], 2 * px + py, c, rows) if i < n else outs[i].at[2 * px + py]
                pltpu.make_async_remote_copy(src_ref=landed, dst_ref=landed, send_sem=send_sems.at[i, k],
                                             recv_sem=recv_sems.at[i, k], device_id=(px, py, c), device_id_type=MESH).wait_recv()
                if i < n:
                    cp = pltpu.make_async_remote_copy(src_ref=landed, dst_ref=landed, send_sem=pass_send.at[i, k],
                                                      recv_sem=pass_recv.at[i, k], device_id=(x, y, 1 - c), device_id_type=MESH)
                    cp.start()
                    passed.append(cp)
        for cp in sent:
            cp.wait_send()
        for cp in passed:
            cp.wait()
        for cp in local:
            cp.wait()

    return pl.pallas_call(
        body, name="weight_allgather",
        in_specs=[ANY] * (n + nw), out_specs=[ANY] * (n + nw), out_shape=outs_shape,
        scratch_shapes=[pltpu.SemaphoreType.DMA((n + nw, 3)), pltpu.SemaphoreType.DMA((n + nw, 3)),
                        pltpu.SemaphoreType.DMA((n, 3)), pltpu.SemaphoreType.DMA((n, 3)),
                        pltpu.SemaphoreType.DMA((n + nw,)), pltpu.SemaphoreType.DMA((n + nw,))],
    )(*shards, *whole)


HBM_SPEC = pl.BlockSpec(memory_space=pltpu.HBM)
SEM_SPEC = pl.BlockSpec(memory_space=pltpu.SEMAPHORE)
DATAFLOW = pltpu.SideEffectType.DATAFLOW_SIDE_EFFECTING


def _in_hbm(a):
    return pltpu.with_memory_space_constraint(a, pltpu.HBM)


def _split_start(name, groups, after):
    spans, arrays = [], []
    for srcs, lands, _, _ in groups:
        spans.append((len(arrays), len(srcs), len(lands)))
        arrays += list(srcs) + list(lands)
    na, ng = len(arrays), len(groups)

    def body(*refs):
        sems, token = refs[na + 1:na + 1 + 2 * ng], refs[-1]
        for g, (_, _, _, copies) in enumerate(groups):
            off, ns, nl = spans[g]
            for src, dst, dev, idx in copies(refs[off:off + ns], refs[off + ns:off + ns + nl]):
                pltpu.make_async_remote_copy(src_ref=src, dst_ref=dst, send_sem=sems[2 * g].at[idx], recv_sem=sems[2 * g + 1].at[idx],
                                             device_id=dev, device_id_type=MESH).start()
        token[...] = jnp.zeros_like(token)

    outs = pl.pallas_call(
        body, name=name,
        in_specs=[HBM_SPEC] * na + [ANY],
        out_specs=[SEM_SPEC] * (2 * ng) + [HBM_SPEC] * na + [pl.BlockSpec(memory_space=pltpu.VMEM)],
        out_shape=[pltpu.SemaphoreType.DMA((n_sems,)) for _, _, n_sems, _ in groups for _ in range(2)]
        + [pltpu.HBM(a.shape, a.dtype) for a in arrays] + [jax.ShapeDtypeStruct((SUBLANES, LANES), F32)],
        input_output_aliases={i: 2 * ng + i for i in range(na)},
        compiler_params=pltpu.CompilerParams(has_side_effects=DATAFLOW),
    )(*[_in_hbm(a) for a in arrays], after)
    started = []
    for g, (off, ns, nl) in enumerate(spans):
        thru = outs[2 * ng + off:2 * ng + off + ns + nl]
        started.append(dict(send=outs[2 * g], recv=outs[2 * g + 1], srcs=list(thru[:ns]), lands=list(thru[ns:]),
                            tile=outs[-1], token=outs[-1][0, 0]))
    return started


def _split_wait(name, started, copies, after):
    n, m = len(started["srcs"]), len(started["lands"])

    def body(*refs):
        src_refs, land_refs = refs[:n], refs[n:n + m]
        send_sem, recv_sem = refs[n + m], refs[n + m + 1]
        for src, dst, dev, idx in copies(src_refs, land_refs):
            cp = pltpu.make_async_remote_copy(src_ref=src, dst_ref=dst, send_sem=send_sem.at[idx], recv_sem=recv_sem.at[idx],
                                              device_id=dev, device_id_type=MESH)
            cp.wait_send()
            cp.wait_recv()

    arrays = started["srcs"] + started["lands"]
    outs = pl.pallas_call(
        body, name=name,
        in_specs=[HBM_SPEC] * (n + m) + [SEM_SPEC, SEM_SPEC, ANY],
        out_specs=[HBM_SPEC] * (n + m),
        out_shape=[pltpu.HBM(a.shape, a.dtype) for a in arrays],
        input_output_aliases={i: i for i in range(n + m)},
        compiler_params=pltpu.CompilerParams(has_side_effects=DATAFLOW),
    )(*arrays, started["send"], started["recv"], after)
    return list(outs)


def _gather_copies(srcs, lands):
    x, y, c = _position()
    chip = 2 * x + y
    targets = [(px, py, c) for px, py in _chip_peers(x, y)] + [(x, y, 1 - c)]
    return [(s, l.at[chip], dev, len(targets) * i + k) for i, (s, l) in enumerate(zip(srcs, lands)) for k, dev in enumerate(targets)]


def _sibling_copies(srcs, lands):
    x, y, c = _position()
    return [(_half_rows(srcs[0], 1 - c, srcs[0].shape[1]), lands[0], (x, y, 1 - c), 0)]


def _exchange_copies(srcs, lands):
    x, y, c = _position()
    return [(srcs[0].at[2 * px + py], lands[0].at[k], (px, py, c), k) for k, (px, py) in enumerate(_chip_peers(x, y))]


def _pair_sum_call(grad, recv, core, name):
    _, h, B = recv.shape
    tr = _row_tile(h, B)

    def body(core_ref, g_ref, r_ref, o_ref, ob_ref):
        s = g_ref[...] + r_ref[...]
        o_ref[...] = s
        ob_ref[...] = s.astype(BF16)

    g_spec = pl.BlockSpec((None, tr, B), lambda q, i, core_ref: (q, core_ref[0] * (h // tr) + i, 0))
    spec = pl.BlockSpec((None, tr, B), lambda q, i, core_ref: (q, i, 0))
    return pl.pallas_call(
        body, name=name,
        grid_spec=pltpu.PrefetchScalarGridSpec(num_scalar_prefetch=1, grid=(N_CHIPS, h // tr), in_specs=[g_spec, spec],
                                               out_specs=[spec, spec]),
        out_shape=[jax.ShapeDtypeStruct(recv.shape, F32), jax.ShapeDtypeStruct(recv.shape, BF16)],
        compiler_params=_params(("parallel", "parallel")),
    )(core, grad, recv)


def _chip_sum_call(partial, recv, chip_core, name):
    _, h, B = recv.shape
    tr = _row_tile(h, B)

    def body(cc_ref, p_ref, r_ref, o_ref):
        o_ref[...] = ((p_ref[...] + r_ref[0].astype(F32)) + r_ref[1].astype(F32)) + r_ref[2].astype(F32)

    return pl.pallas_call(
        body, name=name,
        grid_spec=pltpu.PrefetchScalarGridSpec(
            num_scalar_prefetch=1, grid=(h // tr,),
            in_specs=[pl.BlockSpec((None, tr, B), lambda i, cc_ref: (cc_ref[0], i, 0)),
                      pl.BlockSpec((3, tr, B), lambda i, cc_ref: (0, i, 0))],
            out_specs=pl.BlockSpec((tr, B), lambda i, cc_ref: (cc_ref[1] * (h // tr) + i, 0))),
        out_shape=jax.ShapeDtypeStruct((2 * h, B), F32),
        compiler_params=_params(("parallel",)),
    )(chip_core, partial, recv)


def _sibling_assemble_call(shards, name="grad_sibling_assemble"):
    n = len(shards)

    def body(*refs):
        ins, outs = refs[:n], refs[n:2 * n]
        send_sems, recv_sems = refs[2 * n:]
        x, y, c = _position()
        copies = []
        for i in range(n):
            rows = shards[i].shape[0]
            cp = pltpu.make_async_remote_copy(src_ref=_half_rows(ins[i], c, rows), dst_ref=_half_rows(outs[i], c, rows),
                                              send_sem=send_sems.at[i], recv_sem=recv_sems.at[i],
                                              device_id=(x, y, 1 - c), device_id_type=MESH)
            cp.start()
            copies.append(cp)
        for cp in copies:
            cp.wait()

    return pl.pallas_call(
        body, name=name,
        in_specs=[ANY] * n, out_specs=[ANY] * n,
        out_shape=[jax.ShapeDtypeStruct(s.shape, F32) for s in shards],
        input_output_aliases={i: i for i in range(n)},
        scratch_shapes=[pltpu.SemaphoreType.DMA((n,)), pltpu.SemaphoreType.DMA((n,))],
    )(*shards)


def _small_allreduce_call(packed):
    rows = packed.shape[0]

    def body(x_ref, o_ref, buf_ref, send_sems, recv_sems):
        x, y, c = _position()
        me = 4 * x + 2 * y + c
        buf_ref[me] = x_ref[...]
        copies = []
        for k in range(1, 8):
            peer = (1 - x if k & 4 else x, 1 - y if k & 2 else y, 1 - c if k & 1 else c)
            cp = pltpu.make_async_remote_copy(src_ref=buf_ref.at[me], dst_ref=buf_ref.at[me], send_sem=send_sems.at[k - 1],
                                              recv_sem=recv_sems.at[k - 1], device_id=peer, device_id_type=MESH)
            cp.start()
            copies.append(cp)
        for cp in copies:
            cp.wait()
        acc = buf_ref[0]
        for d in range(1, 8):
            acc = acc + buf_ref[d]
        o_ref[...] = acc

    return pl.pallas_call(
        body, name="small_grad_allreduce",
        in_specs=[pl.BlockSpec(memory_space=pltpu.VMEM)], out_specs=pl.BlockSpec(memory_space=pltpu.VMEM),
        out_shape=jax.ShapeDtypeStruct((rows, LANES), F32),
        scratch_shapes=[pltpu.VMEM((8, rows, LANES), F32), pltpu.SemaphoreType.DMA((7,)), pltpu.SemaphoreType.DMA((7,))],
    )(packed)


def _pack(arrays):
    flat = jnp.concatenate([a.reshape(-1).astype(F32) for a in arrays])
    rows = -(-flat.shape[0] // LANES)
    rows = -(-rows // SUBLANES) * SUBLANES
    flat = jnp.pad(flat, (0, rows * LANES - flat.shape[0]))
    return flat.reshape(rows, LANES)


def _unpack(packed, shapes):
    flat = packed.reshape(-1)
    out, off = [], 0
    for shp in shapes:
        size = int(np.prod(shp))
        out.append(flat[off:off + size].reshape(shp))
        off += size
    return out


def _local_step(xs, target, P, late_weights, on_grad):
    S, D = xs.shape
    qkv_width = 3 * N_HEADS * HEAD_DIM
    glu_col0, gate_col0 = qkv_width, qkv_width + 2 * D
    shard_major = lambda g: g.reshape(N_CHIPS, g.shape[0] // N_CHIPS, g.shape[1])

    h1 = _rms_fwd_call(xs, P["norm_mix_pre"])
    proj = _matmul(h1, P["w_in"], "nn", "proj_in")
    buckets = _bucket_tables()
    bias = _bias_table_call(P["rel_bias"], buckets)
    parts = []
    for g in range(N_GROUPS):
        parts += _attn_fwd_call(proj, bias, g)
    a, a_bf, lse = _attn_merge_call(parts)
    P = dict(P, **late_weights("mix", a_bf))
    y_a = _matmul(a_bf, P["w_attn_out"], "nn", "attn_out")
    c1 = _conv_fwd_call(proj, glu_col0, P["conv_dw_w"], P["conv_dw_b"])
    cact = _ln_silu_call(c1, P["conv_ln_g"], P["conv_ln_b"])
    y_c = _matmul(cact, P["conv_pw_w"], "nn", "conv_pw")
    mixed = _mix_call(proj, gate_col0, P["b_gate"], y_a, y_c)
    out = _matmul(mixed, P["w_out"], "nn", "mix_out")
    x1, h2 = _res1_call(xs, out, P["norm_mix_post"], P["norm_ffn_pre"])
    P = dict(P, **late_weights("ffn", h2))
    u = _matmul(h2, P["w_up"], "nn", "ffn_up")
    f = _ffn_fwd_call(u, P["ffn_conv_w"], P["ffn_conv_b"])
    yff = _matmul(f, P["w_down"], "nn", "ffn_down")
    loss_tile, dx2 = _loss_call(yff, x1, P["norm_ffn_post"], target)

    G = {}
    dyff, G["norm_ffn_post"] = _rms_bwd_call(yff, P["norm_ffn_post"], dx2, "rms_ffn_post_bwd")
    zero = on_grad("w_down", shard_major(_matmul(f, dyff, "tn", "ffn_down_dw")))
    df = _matmul(dyff, P["w_down"], "nt", "ffn_down_dx")
    dug, duv, dwg, dwv, dbg, dbv = _ffn_bwd_call(u, P["ffn_conv_w"], P["ffn_conv_b"] + zero, df)
    G["ffn_conv_w"] = jnp.concatenate([dwg[:FFN_CONV_WIDTH], dwv[:FFN_CONV_WIDTH]], axis=1)
    G["ffn_conv_b"] = jnp.concatenate([dbg, dbv], axis=1)
    du = jnp.concatenate([dug, duv], axis=1)
    zero = on_grad("w_up", _matmul(h2, du, "tn", "ffn_up_dw", out_shards=True))
    dh2 = _matmul(du, P["w_up"], "nt", "ffn_up_dx")
    dx1, dout, G["norm_ffn_pre"], G["norm_mix_post"] = _mid_bwd_call(x1, P["norm_ffn_pre"] + zero, dh2, dx2, out, P["norm_mix_post"])
    zero = on_grad("w_out", shard_major(_matmul(mixed, dout, "tn", "mix_out_dw")))
    dmixed = _matmul(dout, P["w_out"], "nt", "mix_out_dx")
    dya, dyc, dga, dgc, dba, dbc = _mix_bwd_call(dmixed, proj, gate_col0, P["b_gate"] + zero, y_a, y_c)
    G["b_gate"] = jnp.concatenate([dba, dbc], axis=1)
    zero = on_grad("w_attn_out", _matmul(a_bf, dya, "tn", "attn_out_dw", out_shards=True))
    zero = zero + on_grad("conv_pw_w", shard_major(_matmul(cact, dyc, "tn", "conv_pw_dw")))
    da = _matmul(dya, P["w_attn_out"], "nt", "attn_out_dx")
    dcact = _matmul(dyc, P["conv_pw_w"], "nt", "conv_pw_dx")
    dc1, G["conv_ln_g"], G["conv_ln_b"] = _ln_silu_bwd_call(c1, P["conv_ln_g"] + zero, P["conv_ln_b"], dcact)
    dval, dgate, dw_dw, G["conv_dw_b"] = _conv_bwd_call(proj, glu_col0, P["conv_dw_w"], dc1)
    G["conv_dw_w"] = dw_dw[:CONV_WIDTH]
    delta = _attn_delta_call(a, da)
    dqs, dks, dvs, dbs = [], [], [], []
    for g in range(N_GROUPS):
        dq, dk, dv, db = _attn_bwd_call(proj, bias, da, lse, delta, g)
        dqs.append(dq)
        dks.append(dk)
        dvs.append(dv)
        dbs.append(db)
    G["rel_bias"] = _bias_grad_call(jnp.concatenate(dbs, axis=0), buckets)
    dproj = _dproj_call(dqs + dks + dvs, [dval, dgate, dga, dgc])
    zero = on_grad("w_in", _matmul(h1, dproj, "tn", "proj_in_dw", out_shards=True, tm=512))
    dh1 = _matmul(dproj, P["w_in"], "nt", "proj_in_dx")
    grad_x, G["norm_mix_pre"] = _in_bwd_call(xs, P["norm_mix_pre"] + zero, dh1, dx1)
    return loss_tile, grad_x, G


def kernel(x, w_in, b_gate, rel_bias, w_attn_out, conv_dw_w, conv_dw_b, conv_ln_g, conv_ln_b, conv_pw_w, w_out, norm_mix_pre, norm_mix_post, norm_ffn_pre, norm_ffn_post, w_up, ffn_conv_w, ffn_conv_b, w_down, loss_target, m_w_in, m_b_gate, m_rel_bias, m_w_attn_out, m_conv_dw_w, m_conv_dw_b, m_conv_ln_g, m_conv_ln_b, m_conv_pw_w, m_w_out, m_norm_mix_pre, m_norm_mix_post, m_norm_ffn_pre, m_norm_ffn_post, m_w_up, m_ffn_conv_w, m_ffn_conv_b, m_w_down, v_w_in, v_b_gate, v_rel_bias, v_w_attn_out, v_conv_dw_w, v_conv_dw_b, v_conv_ln_g, v_conv_ln_b, v_conv_pw_w, v_w_out, v_norm_mix_pre, v_norm_mix_post, v_norm_ffn_pre, v_norm_ffn_post, v_w_up, v_ffn_conv_w, v_ffn_conv_b, v_w_down):
    weights = dict(w_in=w_in, b_gate=b_gate, rel_bias=rel_bias, w_attn_out=w_attn_out, conv_dw_w=conv_dw_w, conv_dw_b=conv_dw_b,
                   conv_ln_g=conv_ln_g, conv_ln_b=conv_ln_b, conv_pw_w=conv_pw_w, w_out=w_out, norm_mix_pre=norm_mix_pre,
                   norm_mix_post=norm_mix_post, norm_ffn_pre=norm_ffn_pre, norm_ffn_post=norm_ffn_post, w_up=w_up,
                   ffn_conv_w=ffn_conv_w, ffn_conv_b=ffn_conv_b, w_down=w_down)
    m_in = dict(w_in=m_w_in, b_gate=m_b_gate, rel_bias=m_rel_bias, w_attn_out=m_w_attn_out, conv_dw_w=m_conv_dw_w,
                conv_dw_b=m_conv_dw_b, conv_ln_g=m_conv_ln_g, conv_ln_b=m_conv_ln_b, conv_pw_w=m_conv_pw_w, w_out=m_w_out,
                norm_mix_pre=m_norm_mix_pre, norm_mix_post=m_norm_mix_post, norm_ffn_pre=m_norm_ffn_pre,
                norm_ffn_post=m_norm_ffn_post, w_up=m_w_up, ffn_conv_w=m_ffn_conv_w, ffn_conv_b=m_ffn_conv_b, w_down=m_w_down)
    v_in = dict(w_in=v_w_in, b_gate=v_b_gate, rel_bias=v_rel_bias, w_attn_out=v_w_attn_out, conv_dw_w=v_conv_dw_w,
                conv_dw_b=v_conv_dw_b, conv_ln_g=v_conv_ln_g, conv_ln_b=v_conv_ln_b, conv_pw_w=v_conv_pw_w, w_out=v_w_out,
                norm_mix_pre=v_norm_mix_pre, norm_mix_post=v_norm_mix_post, norm_ffn_pre=v_norm_ffn_pre,
                norm_ffn_post=v_norm_ffn_post, w_up=v_w_up, ffn_conv_w=v_ffn_conv_w, ffn_conv_b=v_ffn_conv_b, w_down=v_w_down)
    names = list(weights)
    xi, yi, ci = _position()
    chip = 2 * xi + yi
    core_arr = jnp.reshape(ci, (1,)).astype(jnp.int32)

    xs = x[0]
    target = loss_target[0]
    S, D = xs.shape

    big = ["w_in", "w_attn_out", "conv_pw_w", "w_out", "w_up", "w_down"]
    row_sharded = ("conv_pw_w", "w_out", "w_down")
    bf16_shard = {k: weights[k][0].astype(BF16) for k in big}
    natural = lambda k, g: g.reshape(-1, g.shape[2]) if k in row_sharded else g
    w_in_full, dw4, fc4 = _allgather_call([bf16_shard["w_in"]], [conv_dw_w[0], ffn_conv_w[0]])
    late_sets = dict(mix=["w_attn_out", "conv_pw_w", "w_out"], ffn=["w_up", "w_down"])
    late_groups = []
    for keys in late_sets.values():
        srcs = [bf16_shard[k] for k in keys]
        late_groups.append((srcs, [lax.empty((N_CHIPS,) + s.shape, BF16) for s in srcs], 4 * len(keys), _gather_copies))
    started = dict(zip(late_sets, _split_start("gather_late_start", late_groups, w_in_full)))
    launched = started["mix"]["token"]

    def late_weights(tag, after):
        landed = _split_wait(f"gather_{tag}_wait", started[tag], _gather_copies, after)[len(late_sets[tag]):]
        return {k: natural(k, g) for k, g in zip(late_sets[tag], landed)}

    chip_core = jnp.stack([chip, ci]).astype(jnp.int32)
    exchanging, pending = {}, {}

    def launch(tag, g3, after):
        keys, groups, partial = [], [], {}
        for k in list(exchanging):
            gk, r1 = _split_wait(f"sibling_exchange_wait_{k}", exchanging.pop(k), _sibling_copies, after)
            partial[k], s16 = _pair_sum_call(gk, r1, core_arr, f"pair_sum_{k}")
            keys.append(k)
            groups.append(([s16], [lax.empty((3,) + s16.shape[1:], BF16)], 3, _exchange_copies))
        if g3 is not None:
            groups.append(([g3], [lax.empty((N_CHIPS, g3.shape[1] // 2, g3.shape[2]), F32)], 1, _sibling_copies))
        begun = _split_start(f"grad_exchange_start_{tag}", groups, core_arr)
        for k, st in zip(keys, begun):
            pending[k] = (partial[k], st)
        if g3 is not None:
            exchanging[tag] = begun[-1]
        return begun[0]["token"]

    def on_grad(k, g3):
        return launch(k, g3, g3[0, :SUBLANES, :LANES])

    def pair_up(after):
        return launch("last", None, after)

    def finish(keys, after, tag):
        halves = []
        for k in keys:
            s32, st = pending[k]
            recv2 = _split_wait(f"chip_exchange_wait_{k}", st, _exchange_copies, after)[1]
            halves.append(_chip_sum_call(s32, recv2, chip_core, f"chip_sum_{k}"))
        return dict(zip(keys, _sibling_assemble_call(halves, f"grad_sibling_assemble_{tag}")))

    P = dict(w_in=w_in_full, conv_dw_w=jnp.concatenate(list(dw4), axis=1), ffn_conv_w=jnp.concatenate(list(fc4), axis=1),
             b_gate=b_gate, rel_bias=rel_bias, conv_dw_b=conv_dw_b, conv_ln_g=conv_ln_g, conv_ln_b=conv_ln_b,
             norm_mix_pre=norm_mix_pre + launched, norm_mix_post=norm_mix_post, norm_ffn_pre=norm_ffn_pre,
             norm_ffn_post=norm_ffn_post, ffn_conv_b=ffn_conv_b)
    loss_tile, grad_x, G = _local_step(xs, target, P, late_weights, on_grad)

    small = [k for k in names if k not in big]
    packed = _pack([loss_tile[:1]] + [G[k] for k in small])
    summed_block = _small_allreduce_call(packed)
    summed_block = summed_block + pair_up(summed_block[:SUBLANES])
    loss_row, *summed = _unpack(summed_block, [(1, LANES)] + [G[k].shape for k in small])
    loss = loss_row[0, 0]
    reduced = {}
    for k, gsum in zip(small, summed):
        if k in ("conv_dw_w", "ffn_conv_w"):
            cols = weights[k].shape[2]
            reduced[k] = lax.dynamic_slice_in_dim(gsum, chip * cols, cols, axis=1)
        else:
            reduced[k] = gsum

    grads, deltas, new_m, new_v = {}, {}, {}, {}

    def update(keys):
        for k in keys:
            d, mn, vn = _adamw_call(weights[k][0], reduced[k], m_in[k][0], v_in[k][0], f"adamw_{k}")
            grads[k], deltas[k], new_m[k], new_v[k] = reduced[k][None], d[None], mn[None], vn[None]

    others = [k for k in big if k != "w_in"]
    reduced.update(finish(others, summed_block, "others"))
    update(others)
    reduced.update(finish(["w_in"], deltas["w_up"], "w_in"))
    update(["w_in"])
    flat2 = lambda t: t.reshape(-1, t.shape[-1]) if t.ndim == 3 else t
    ds, mns, vns = _adamw_small_call([flat2(weights[k]) for k in small], [reduced[k] for k in small],
                                     [flat2(m_in[k]) for k in small], [flat2(v_in[k]) for k in small])
    for k, dk_, mk, vk in zip(small, ds, mns, vns):
        shape = weights[k].shape
        grads[k], deltas[k], new_m[k], new_v[k] = reduced[k].reshape(shape), dk_.reshape(shape), mk.reshape(shape), vk.reshape(shape)

    return (loss, grad_x[None], *[grads[k] for k in names], *[deltas[k] for k in names],
            *[new_m[k] for k in names], *[new_v[k] for k in names])
```

```python
import functools
import math

import jax
import jax.numpy as jnp
import numpy as np
from jax import lax
from jax.experimental import pallas as pl
from jax.experimental.pallas import tpu as pltpu

F32 = jnp.float32
BF16 = jnp.bfloat16
MESH = pl.DeviceIdType.MESH

HEAD_DIM = 128
HEADS_PER_GROUP = 4
DILATED_PATTERNS = ((128, 1), (512, 4), (2048, 16))
N_GROUPS = 3
N_HEADS = N_GROUPS * HEADS_PER_GROUP
SPAN = 128
GROUP_WIDTH = HEADS_PER_GROUP * HEAD_DIM
CONV_WIDTH = 31
FFN_CONV_WIDTH = 3
N_BUCKETS = 32
MAX_DISTANCE = 2048
RMS_EPS = 1e-6
LN_EPS = 1e-5
NEG_INF = -1e30
ADAM_LR = 0.001
ADAM_B1 = 0.9
ADAM_B2 = 0.999
ADAM_EPS = 1e-08
ADAM_WD = 0.01
ADAM_STEP = 10

LANES = 128
SUBLANES = 8
ROW_TILE = 256
TIME_BLOCK = 128
CONV_PAD = 32
FFN_PAD = 8
VMEM_LIMIT = 56 << 20


def _params(sem=None, vmem=None):
    kw = {}
    if sem is not None:
        kw["dimension_semantics"] = sem
    if vmem is not None:
        kw["vmem_limit_bytes"] = vmem
    return pltpu.CompilerParams(**kw)


def _pick(n, cands):
    for c in cands:
        if n % c == 0:
            return c
    return n


ELEMENTWISE_TILE_BYTES = 3 << 19


def _row_tile(rows, cols):
    for align in (16, SUBLANES):
        fits = [t for t in range(align, rows + 1, align) if rows % t == 0 and t * cols * 4 <= ELEMENTWISE_TILE_BYTES]
        if fits:
            return max(fits)
    return SUBLANES


N_CHIPS = 4
M_TILES = (1024, 1408, 512, 256, 128)
N_TILES = (1024, 512, 1408, 256, 128)
K_TILES = (2176, 2048, 1408, 1024, 512, 256, 128)


def _matmul(a, b, mode, name, out_shards=False, tm=None):
    assert a.dtype == BF16 and b.dtype == BF16, (name, a.dtype, b.dtype)
    b3 = b.ndim == 3
    tn = tk = None
    halves = None
    if mode == "nn":
        M, K = a.shape
        N = b.shape[-1] * (N_CHIPS if b3 else 1)
        tn = b.shape[-1] if b3 else None
    elif mode == "nt":
        if a.ndim == 3:
            halves = a.shape[2]
        M, K = a.shape[-2], a.shape[-1] * (a.shape[0] if a.ndim == 3 else 1)
        N = b.shape[-2]
        tk = b.shape[-1] if b3 else None
    else:
        if b3:
            halves = b.shape[2]
        K, M = a.shape
        N = b.shape[-1] * (b.shape[0] if b3 else 1)
        tn = N // N_CHIPS if out_shards else None
    tm = tm or _pick(M, M_TILES)
    tn = tn or _pick(N, N_TILES)
    tk = tk or _pick(K, K_TILES)
    nk = K // tk
    dn = {"nn": (((1,), (0,)), ((), ())), "nt": (((1,), (1,)), ((), ())), "tn": (((0,), (0,)), ((), ()))}[mode]

    def body(a_ref, b_ref, o_ref):
        if nk == 1:
            o_ref[...] = lax.dot_general(a_ref[...], b_ref[...], dn, preferred_element_type=F32)
        else:
            @pl.when(pl.program_id(2) == 0)
            def _():
                o_ref[...] = jnp.zeros_like(o_ref)

            o_ref[...] += lax.dot_general(a_ref[...], b_ref[...], dn, preferred_element_type=F32)

    if mode == "tn":
        a_spec = pl.BlockSpec((tk, tm), lambda i, j, k: (k, i))
    elif halves:
        per = halves // tk
        a_spec = pl.BlockSpec((None, tm, tk), lambda i, j, k: (k // per, i, k % per))
    else:
        a_spec = pl.BlockSpec((tm, tk), lambda i, j, k: (i, k))
    if mode == "nn":
        b_spec = pl.BlockSpec((None, tk, tn), lambda i, j, k: (j, k, 0)) if b3 else pl.BlockSpec((tk, tn), lambda i, j, k: (k, j))
    elif mode == "nt":
        b_spec = pl.BlockSpec((None, tn, tk), lambda i, j, k: (k, j, 0)) if b3 else pl.BlockSpec((tn, tk), lambda i, j, k: (j, k))
    elif halves:
        per = halves // tn
        b_spec = pl.BlockSpec((None, tk, tn), lambda i, j, k: (j // per, k, j % per))
    else:
        b_spec = pl.BlockSpec((tk, tn), lambda i, j, k: (k, j))
    if out_shards:
        out_spec = pl.BlockSpec((None, tm, tn), lambda i, j, k: (j, i, 0))
        out_shape = jax.ShapeDtypeStruct((N_CHIPS, M, tn), F32)
    else:
        out_spec = pl.BlockSpec((tm, tn), lambda i, j, k: (i, j))
        out_shape = jax.ShapeDtypeStruct((M, N), F32)
    return pl.pallas_call(
        body, name=name, grid=(M // tm, N // tn, nk),
        in_specs=[a_spec, b_spec], out_specs=out_spec, out_shape=out_shape,
        compiler_params=_params(("parallel", "parallel", "arbitrary"), VMEM_LIMIT),
    )(a, b)


def _rms(x, g):
    r = lax.rsqrt(jnp.mean(x * x, axis=-1, keepdims=True) + RMS_EPS)
    return x * r * g


def _rms_bwd(x, g, dy):
    r = lax.rsqrt(jnp.mean(x * x, axis=-1, keepdims=True) + RMS_EPS)
    n = x * r
    dn = dy * g
    dx = r * (dn - n * jnp.mean(dn * n, axis=-1, keepdims=True))
    return dx, jnp.sum(dy * n, axis=0, keepdims=True)


def _sigmoid(x):
    return 1.0 / (1.0 + jnp.exp(-x))


_GELU_C = math.sqrt(2.0 / math.pi)


def _gelu(x):
    return 0.5 * x * (1.0 + jnp.tanh(_GELU_C * (x + 0.044715 * x * x * x)))


def _gelu_and_grad(x):
    x2 = x * x
    t = jnp.tanh(_GELU_C * x * (1.0 + 0.044715 * x2))
    half = 0.5 * (1.0 + t)
    return x * half, half + (0.5 * _GELU_C) * x * (1.0 - t * t) * (1.0 + (3.0 * 0.044715) * x2)


def _row_spec(width, col_block=0):
    return pl.BlockSpec((ROW_TILE, width), lambda i: (i, col_block))


def _vec_spec(width, col_block=0):
    return pl.BlockSpec((1, width), lambda i: (0, col_block))


def _accumulate(ref, part):
    @pl.when(pl.program_id(0) == 0)
    def _():
        ref[...] = part

    @pl.when(pl.program_id(0) > 0)
    def _():
        ref[...] += part


def _rms_fwd_call(x, g):
    S, D = x.shape

    def body(x_ref, g_ref, h_ref):
        h_ref[...] = _rms(x_ref[...], g_ref[...]).astype(BF16)

    return pl.pallas_call(
        body, name="rms_mix_pre", grid=(S // ROW_TILE,),
        in_specs=[_row_spec(D), _vec_spec(D)], out_specs=_row_spec(D),
        out_shape=jax.ShapeDtypeStruct((S, D), BF16),
        compiler_params=_params(("parallel",)),
    )(x, g)


def _ln_silu_call(c1, g, b):
    S, C = c1.shape

    def body(c_ref, g_ref, b_ref, o_ref):
        xv = c_ref[...]
        mu = jnp.mean(xv, axis=-1, keepdims=True)
        xc = xv - mu
        var = jnp.mean(xc * xc, axis=-1, keepdims=True)
        z = xc * lax.rsqrt(var + LN_EPS) * g_ref[...] + b_ref[...]
        o_ref[...] = (z * _sigmoid(z)).astype(BF16)

    return pl.pallas_call(
        body, name="conv_ln_silu", grid=(S // ROW_TILE,),
        in_specs=[_row_spec(C), _vec_spec(C), _vec_spec(C)], out_specs=_row_spec(C),
        out_shape=jax.ShapeDtypeStruct((S, C), BF16),
        compiler_params=_params(("parallel",)),
    )(c1, g, b)


def _ln_silu_bwd_call(c1, g, b, dc):
    S, C = c1.shape

    def body(c_ref, g_ref, b_ref, dc_ref, dx_ref, dg_ref, db_ref):
        xv = c_ref[...]
        mu = jnp.mean(xv, axis=-1, keepdims=True)
        xc = xv - mu
        rs = lax.rsqrt(jnp.mean(xc * xc, axis=-1, keepdims=True) + LN_EPS)
        xh = xc * rs
        z = xh * g_ref[...] + b_ref[...]
        sg = _sigmoid(z)
        dz = dc_ref[...] * (sg * (1.0 + z * (1.0 - sg)))
        dxh = dz * g_ref[...]
        dx_ref[...] = rs * (dxh - jnp.mean(dxh, axis=-1, keepdims=True) - xh * jnp.mean(dxh * xh, axis=-1, keepdims=True))
        _accumulate(dg_ref, jnp.sum(dz * xh, axis=0, keepdims=True))
        _accumulate(db_ref, jnp.sum(dz, axis=0, keepdims=True))

    return pl.pallas_call(
        body, name="conv_ln_silu_bwd", grid=(S // ROW_TILE,),
        in_specs=[_row_spec(C), _vec_spec(C), _vec_spec(C), _row_spec(C)],
        out_specs=[_row_spec(C), _vec_spec(C), _vec_spec(C)],
        out_shape=[jax.ShapeDtypeStruct((S, C), F32), jax.ShapeDtypeStruct((1, C), F32), jax.ShapeDtypeStruct((1, C), F32)],
        compiler_params=_params(("arbitrary",)),
    )(c1, g, b, dc)


def _mix_call(proj, gate_col0, b_gate, y_a, y_c):
    S, D = y_a.shape
    w = 512
    nc = D // w
    ga0, gc0 = gate_col0 // w, (gate_col0 + D) // w

    def body(ga_ref, gc_ref, ba_ref, bc_ref, ya_ref, yc_ref, o_ref):
        o_ref[...] = (_sigmoid(ga_ref[...] + ba_ref[...]) * ya_ref[...]
                      + _sigmoid(gc_ref[...] + bc_ref[...]) * yc_ref[...]).astype(BF16)

    tile = lambda off: pl.BlockSpec((ROW_TILE, w), lambda i, j: (i, off + j))
    vec = lambda off: pl.BlockSpec((1, w), lambda i, j: (0, off + j))
    return pl.pallas_call(
        body, name="gate_mix", grid=(S // ROW_TILE, nc),
        in_specs=[tile(ga0), tile(gc0), vec(0), vec(nc), tile(0), tile(0)],
        out_specs=tile(0), out_shape=jax.ShapeDtypeStruct((S, D), BF16),
        compiler_params=_params(("parallel", "parallel")),
    )(proj, proj, b_gate, b_gate, y_a, y_c)


def _mix_bwd_call(dmixed, proj, gate_col0, b_gate, y_a, y_c):
    S, D = y_a.shape
    w = 512
    nc = D // w
    ga0, gc0 = gate_col0 // w, (gate_col0 + D) // w

    def body(dm_ref, ga_ref, gc_ref, ba_ref, bc_ref, ya_ref, yc_ref, dya_ref, dyc_ref, dga_ref, dgc_ref, dba_ref, dbc_ref):
        dm = dm_ref[...]
        sa = _sigmoid(ga_ref[...] + ba_ref[...])
        sc = _sigmoid(gc_ref[...] + bc_ref[...])
        dya_ref[...] = (dm * sa).astype(BF16)
        dyc_ref[...] = (dm * sc).astype(BF16)
        dga = dm * ya_ref[...] * sa * (1.0 - sa)
        dgc = dm * yc_ref[...] * sc * (1.0 - sc)
        dga_ref[...] = dga.astype(BF16)
        dgc_ref[...] = dgc.astype(BF16)
        pa = jnp.sum(dga, axis=0, keepdims=True)
        pc = jnp.sum(dgc, axis=0, keepdims=True)

        @pl.when(pl.program_id(1) == 0)
        def _():
            dba_ref[...] = pa
            dbc_ref[...] = pc

        @pl.when(pl.program_id(1) > 0)
        def _():
            dba_ref[...] += pa
            dbc_ref[...] += pc

    tile = lambda off: pl.BlockSpec((ROW_TILE, w), lambda j, i: (i, off + j))
    vec = lambda off: pl.BlockSpec((1, w), lambda j, i: (0, off + j))
    return pl.pallas_call(
        body, name="gate_mix_bwd", grid=(nc, S // ROW_TILE),
        in_specs=[tile(0), tile(ga0), tile(gc0), vec(0), vec(nc), tile(0), tile(0)],
        out_specs=[tile(0), tile(0), tile(0), tile(0), vec(0), vec(0)],
        out_shape=[jax.ShapeDtypeStruct((S, D), BF16)] * 4 + [
                   jax.ShapeDtypeStruct((1, D), F32), jax.ShapeDtypeStruct((1, D), F32)],
        compiler_params=_params(("parallel", "arbitrary")),
    )(dmixed, proj, proj, b_gate, b_gate, y_a, y_c)


def _res1_call(x, out, g_post, g_pre):
    S, D = x.shape

    def body(x_ref, o_ref, gp_ref, gq_ref, x1_ref, h2_ref):
        x1 = x_ref[...] + _rms(o_ref[...], gp_ref[...])
        x1_ref[...] = x1
        h2_ref[...] = _rms(x1, gq_ref[...]).astype(BF16)

    return pl.pallas_call(
        body, name="residual_mix", grid=(S // ROW_TILE,),
        in_specs=[_row_spec(D), _row_spec(D), _vec_spec(D), _vec_spec(D)],
        out_specs=[_row_spec(D), _row_spec(D)],
        out_shape=[jax.ShapeDtypeStruct((S, D), F32), jax.ShapeDtypeStruct((S, D), BF16)],
        compiler_params=_params(("parallel",)),
    )(x, out, g_post, g_pre)


def _loss_call(y, x1, g_post, target):
    S, D = y.shape

    def body(y_ref, x1_ref, g_ref, t_ref, loss_ref, dx_ref):
        err = x1_ref[...] + _rms(y_ref[...], g_ref[...]) - t_ref[...]
        dx_ref[...] = err * (1.0 / D)
        part = 0.5 * jnp.sum(jnp.mean(err * err, axis=-1, keepdims=True), axis=0, keepdims=True)
        _accumulate(loss_ref, jnp.broadcast_to(part, (SUBLANES, LANES)))

    return pl.pallas_call(
        body, name="residual_ffn_loss", grid=(S // ROW_TILE,),
        in_specs=[_row_spec(D), _row_spec(D), _vec_spec(D), _row_spec(D)],
        out_specs=[pl.BlockSpec((SUBLANES, LANES), lambda i: (0, 0)), _row_spec(D)],
        out_shape=[jax.ShapeDtypeStruct((SUBLANES, LANES), F32), jax.ShapeDtypeStruct((S, D), F32)],
        compiler_params=_params(("arbitrary",)),
    )(y, x1, g_post, target)


def _rms_bwd_call(x, g, dy, name):
    S, D = x.shape

    def body(x_ref, g_ref, dy_ref, dx_ref, dg_ref):
        dx, dg = _rms_bwd(x_ref[...], g_ref[...], dy_ref[...])
        dx_ref[...] = dx.astype(BF16)
        _accumulate(dg_ref, dg)

    return pl.pallas_call(
        body, name=name, grid=(S // ROW_TILE,),
        in_specs=[_row_spec(D), _vec_spec(D), _row_spec(D)],
        out_specs=[_row_spec(D), _vec_spec(D)],
        out_shape=[jax.ShapeDtypeStruct((S, D), BF16), jax.ShapeDtypeStruct((1, D), F32)],
        compiler_params=_params(("arbitrary",)),
    )(x, g, dy)


def _mid_bwd_call(x1, g_pre, dh2, dx2, out, g_post):
    S, D = x1.shape

    def body(x1_ref, gq_ref, dh_ref, dx2_ref, o_ref, gp_ref, dx1_ref, do_ref, dgq_ref, dgp_ref):
        d, dgq = _rms_bwd(x1_ref[...], gq_ref[...], dh_ref[...])
        dx1 = dx2_ref[...] + d
        dx1_ref[...] = dx1
        do, dgp = _rms_bwd(o_ref[...], gp_ref[...], dx1)
        do_ref[...] = do.astype(BF16)
        _accumulate(dgq_ref, dgq)
        _accumulate(dgp_ref, dgp)

    return pl.pallas_call(
        body, name="residual_mix_bwd", grid=(S // ROW_TILE,),
        in_specs=[_row_spec(D), _vec_spec(D), _row_spec(D), _row_spec(D), _row_spec(D), _vec_spec(D)],
        out_specs=[_row_spec(D), _row_spec(D), _vec_spec(D), _vec_spec(D)],
        out_shape=[jax.ShapeDtypeStruct((S, D), F32), jax.ShapeDtypeStruct((S, D), BF16)] + [jax.ShapeDtypeStruct((1, D), F32)] * 2,
        compiler_params=_params(("arbitrary",)),
    )(x1, g_pre, dh2, dx2, out, g_post)


def _in_bwd_call(x, g, dh1, dx1):
    S, D = x.shape

    def body(x_ref, g_ref, dh_ref, dx1_ref, gx_ref, dg_ref):
        d, dg = _rms_bwd(x_ref[...], g_ref[...], dh_ref[...])
        gx_ref[...] = dx1_ref[...] + d
        _accumulate(dg_ref, dg)

    return pl.pallas_call(
        body, name="rms_mix_pre_bwd", grid=(S // ROW_TILE,),
        in_specs=[_row_spec(D), _vec_spec(D), _row_spec(D), _row_spec(D)],
        out_specs=[_row_spec(D), _vec_spec(D)],
        out_shape=[jax.ShapeDtypeStruct((S, D), F32), jax.ShapeDtypeStruct((1, D), F32)],
        compiler_params=_params(("arbitrary",)),
    )(x, g, dh1, dx1)


def _bucket_table(dilation):
    qi = np.arange(SPAN)[:, None]
    ki = np.arange(2 * SPAN)[None, :]
    dist = np.maximum(qi + SPAN - ki, 0) * dilation
    max_exact = N_BUCKETS // 2
    d = np.maximum(dist, 1).astype(np.float64)
    large = max_exact + (np.log(d / max_exact) / math.log(MAX_DISTANCE / max_exact) * (N_BUCKETS - max_exact)).astype(np.int32)
    large = np.minimum(large, N_BUCKETS - 1)
    return np.where(dist < max_exact, dist, large).astype(np.int32)


def _bucket_tables():
    return jnp.asarray(np.stack([_bucket_table(r) for _, r in DILATED_PATTERNS]))


def _bias_table_call(rel_bias, buckets):
    def body(rb_ref, bk_ref, o_ref):
        for h in range(N_HEADS):
            bk = bk_ref[h // HEADS_PER_GROUP]

            def step(b, acc):
                return jnp.where(bk == b, rb_ref[b, h], acc)

            o_ref[h] = lax.fori_loop(0, N_BUCKETS, step, jnp.zeros((SPAN, 2 * SPAN), F32))

    return pl.pallas_call(
        body, name="rel_bias_table",
        in_specs=[pl.BlockSpec(memory_space=pltpu.SMEM), pl.BlockSpec(memory_space=pltpu.VMEM)],
        out_specs=pl.BlockSpec(memory_space=pltpu.VMEM),
        out_shape=jax.ShapeDtypeStruct((N_HEADS, SPAN, 2 * SPAN), F32),
    )(rel_bias, buckets)


def _bias_grad_call(dbias, buckets):
    def body(db_ref, bk_ref, o_ref, rows_ref):
        for h in range(N_HEADS):
            bk = bk_ref[h // HEADS_PER_GROUP]
            dv = db_ref[h]

            def step(b, carry):
                rows_ref[h, b] = jnp.sum(jnp.where(bk == b, dv, 0.0), axis=0, keepdims=True)
                return carry

            lax.fori_loop(0, N_BUCKETS, step, 0)
        o_ref[...] = jnp.sum(rows_ref[...], axis=-1, keepdims=True)

    out = pl.pallas_call(
        body, name="rel_bias_grad",
        in_specs=[pl.BlockSpec(memory_space=pltpu.VMEM), pl.BlockSpec(memory_space=pltpu.VMEM)],
        out_specs=pl.BlockSpec(memory_space=pltpu.VMEM),
        out_shape=jax.ShapeDtypeStruct((N_HEADS, N_BUCKETS, 1, 1), F32),
        scratch_shapes=[pltpu.VMEM((N_HEADS, N_BUCKETS, 1, 2 * SPAN), F32)],
    )(dbias, buckets)
    return out.reshape(N_HEADS, N_BUCKETS).T


def _dot_nt(a, b):
    return lax.dot_general(a, b, (((1,), (1,)), ((), ())), preferred_element_type=F32)


def _dot_nn(a, b):
    return lax.dot_general(a, b, (((1,), (0,)), ((), ())), preferred_element_type=F32)


def _dot_tn(a, b):
    return lax.dot_general(a, b, (((0,), (0,)), ((), ())), preferred_element_type=F32)


def _band_masks(n, nb):
    qi = lax.broadcasted_iota(jnp.int32, (SPAN, SPAN), 0)
    ki = lax.broadcasted_iota(jnp.int32, (SPAN, SPAN), 1)
    prev_ok = jnp.logical_and(ki >= qi, n > 0)
    cur_ok = ki <= qi
    next_ok = jnp.logical_and(ki >= qi, n < nb - 1)
    return prev_ok, cur_ok, next_ok


def _wide_band_mask(n):
    qi = lax.broadcasted_iota(jnp.int32, (SPAN, 2 * SPAN), 0)
    ki = lax.broadcasted_iota(jnp.int32, (SPAN, 2 * SPAN), 1)
    prev_ok = jnp.logical_and(jnp.logical_and(ki < SPAN, ki >= qi), n > 0)
    cur_ok = jnp.logical_and(ki >= SPAN, ki - SPAN <= qi)
    return jnp.logical_or(prev_ok, cur_ok)


def _attn_plan(S, group):
    r = DILATED_PATTERNS[group][1]
    hp, per = (HEADS_PER_GROUP, 1) if r == 1 else (2, 4)
    return r, S // (r * SPAN), hp, per


def _residue_rows(rho, r):
    return slice(None) if r == 1 else pl.ds(rho, SPAN, stride=r)


def _for_residues(r, per, fn):
    if r == per:
        for u in range(per):
            fn(u)
        return

    def step(i, carry):
        for u in range(per):
            fn(i * per + u)
        return carry

    lax.fori_loop(0, r // per, step, 0)


def _attn_fwd_call(proj, bias, group):
    S = proj.shape[0]
    r, nb, hp, per = _attn_plan(S, group)
    scale = HEAD_DIM ** -0.5
    kinds = ("q", "kp", "kc", "vp", "vc") if nb > 1 else ("q", "kc", "vc")

    def body(*refs):
        ins = {kind: refs[i * hp:(i + 1) * hp] for i, kind in enumerate(kinds)}
        b_ref, o_ref, lse_ref = refs[len(kinds) * hp:]
        n = pl.program_id(1)
        prev_ok, cur_ok, _ = _band_masks(n, nb)

        band_ok = _wide_band_mask(n) if nb > 1 else cur_ok

        def residue(rho):
            rows = _residue_rows(rho, r)
            for j in range(hp):
                get = lambda kind: ins[kind][j][rows, :].astype(BF16)
                q = get("q")
                if nb > 1:
                    keys, vals, bias_j = jnp.concatenate([get("kp"), get("kc")], axis=0), jnp.concatenate([get("vp"), get("vc")], axis=0), b_ref[j]
                else:
                    keys, vals, bias_j = get("kc"), get("vc"), b_ref[j, :, SPAN:]
                s = jnp.where(band_ok, _dot_nt(q, keys) * scale + bias_j, NEG_INF)
                m = jnp.max(s, axis=-1, keepdims=True)
                p = jnp.exp(s - m)
                den = jnp.sum(p, axis=-1, keepdims=True)
                o_ref[j, rows, :] = _dot_nn(p.astype(BF16), vals) / den
                lse_ref[j, rows, :] = jnp.broadcast_to(m + jnp.log(den), (SPAN, HEAD_DIM))

        _for_residues(r, per, residue)

    in_specs = [_head_spec(r, nb, hp, kind, group, jj) for kind in kinds for jj in range(hp)]
    in_specs.append(pl.BlockSpec((hp, SPAN, 2 * SPAN), lambda j, n: (group * (HEADS_PER_GROUP // hp) + j, 0, 0)))
    out = pl.BlockSpec((hp, r * SPAN, HEAD_DIM), lambda j, n: (j, n, 0))
    return pl.pallas_call(
        body, name=f"attn_fwd_g{group}", grid=(HEADS_PER_GROUP // hp, nb),
        in_specs=in_specs, out_specs=[out] * 2,
        out_shape=[jax.ShapeDtypeStruct((HEADS_PER_GROUP, S, HEAD_DIM), F32)] * 2,
        compiler_params=_params(("parallel", "parallel"), VMEM_LIMIT),
    )(*([proj] * (len(in_specs) - 1)), bias)


_PROJ_PART = dict(q=0, qn=0, kp=1, kc=1, vp=2, vc=2)


def _head_spec(r, nb, hp, kind, group, jj):
    if kind in _PROJ_PART:
        base = (_PROJ_PART[kind] * N_GROUPS + group) * HEADS_PER_GROUP
    else:
        base = 0
    if kind.endswith("p"):
        row = lambda n: jnp.maximum(n - 1, 0)
    elif kind.endswith("n"):
        row = lambda n: jnp.minimum(n + 1, nb - 1)
    else:
        row = lambda n: n
    return pl.BlockSpec((r * SPAN, HEAD_DIM), lambda j, n: (row(n), base + j * hp + jj))


def _attn_merge_call(parts):
    S = parts[0].shape[1]

    def body(o1, s1, o2, s2, o3, s3, a_ref, ab_ref, lse_ref):
        for j in range(HEADS_PER_GROUP):
            sl = slice(j * HEAD_DIM, (j + 1) * HEAD_DIM)
            mx = jnp.maximum(jnp.maximum(s1[j], s2[j]), s3[j])
            w1 = jnp.exp(s1[j] - mx)
            w2 = jnp.exp(s2[j] - mx)
            w3 = jnp.exp(s3[j] - mx)
            den = w1 + w2 + w3
            a = (w1 * o1[j] + w2 * o2[j] + w3 * o3[j]) / den
            a_ref[:, sl] = a
            ab_ref[:, sl] = a.astype(BF16)
            lse_ref[:, sl] = mx + jnp.log(den)

    heads = pl.BlockSpec((HEADS_PER_GROUP, ROW_TILE, HEAD_DIM), lambda i: (0, i, 0))
    return pl.pallas_call(
        body, name="attn_merge", grid=(S // ROW_TILE,),
        in_specs=[heads] * 6, out_specs=[_row_spec(GROUP_WIDTH)] * 3,
        out_shape=[jax.ShapeDtypeStruct((S, GROUP_WIDTH), F32), jax.ShapeDtypeStruct((S, GROUP_WIDTH), BF16),
                   jax.ShapeDtypeStruct((S, GROUP_WIDTH), F32)],
        compiler_params=_params(("parallel",)),
    )(*parts)


def _attn_delta_call(a, da):
    S = a.shape[0]

    def body(a_ref, da_ref, d_ref):
        for j in range(HEADS_PER_GROUP):
            sl = slice(j * HEAD_DIM, (j + 1) * HEAD_DIM)
            d = jnp.sum(a_ref[:, sl] * da_ref[:, sl], axis=-1, keepdims=True)
            d_ref[:, sl] = jnp.broadcast_to(d, (ROW_TILE, HEAD_DIM))

    return pl.pallas_call(
        body, name="attn_delta", grid=(S // ROW_TILE,),
        in_specs=[_row_spec(GROUP_WIDTH)] * 2, out_specs=_row_spec(GROUP_WIDTH),
        out_shape=jax.ShapeDtypeStruct((S, GROUP_WIDTH), F32),
        compiler_params=_params(("parallel",)),
    )(a, da)


def _attn_bwd_call(proj, bias, da, lse, delta, group):
    S = proj.shape[0]
    r, nb, hp, per = _attn_plan(S, group)
    scale = HEAD_DIM ** -0.5
    kinds = ("q", "qn", "kp", "kc", "vp", "vc", "da", "dan", "lse", "lsen", "dl", "dln") if nb > 1 else ("q", "kc", "vc", "da", "lse", "dl")
    source = dict(da=da, dan=da, lse=lse, lsen=lse, dl=delta, dln=delta)

    def body(*refs):
        ins = {kind: refs[i * hp:(i + 1) * hp] for i, kind in enumerate(kinds)}
        b_ref, dq_ref, dk_ref, dv_ref, db_ref = refs[len(kinds) * hp:]
        n = pl.program_id(1)
        prev_ok, cur_ok, next_ok = _band_masks(n, nb)

        @pl.when(n == 0)
        def _():
            db_ref[...] = jnp.zeros_like(db_ref)

        band_ok = _wide_band_mask(n) if nb > 1 else cur_ok

        def residue(rho):
            rows = _residue_rows(rho, r)
            for j in range(hp):
                get = lambda kind: ins[kind][j][rows, :]
                q = get("q").astype(BF16)
                kc = get("kc").astype(BF16)
                vc = get("vc").astype(BF16)
                dav = get("da").astype(BF16)
                lse_q, dl_q = get("lse"), get("dl")
                if nb == 1:
                    pc = jnp.exp(jnp.where(cur_ok, _dot_nt(q, kc) * scale + b_ref[j, :, SPAN:], NEG_INF) - lse_q)
                    dsc = pc * (_dot_nt(dav, vc) - dl_q)
                    dsc_b = dsc.astype(BF16)
                    dq = _dot_nn(dsc_b, kc)
                    dk = _dot_tn(dsc_b, q)
                    dv = _dot_tn(pc.astype(BF16), dav)
                    db_ref[j, :, SPAN:] += dsc
                else:
                    qn = get("qn").astype(BF16)
                    dan = get("dan").astype(BF16)
                    keys = jnp.concatenate([get("kp").astype(BF16), kc], axis=0)
                    vals = jnp.concatenate([get("vp").astype(BF16), vc], axis=0)
                    wide = lambda t: jnp.concatenate([t, t], axis=1)
                    p = jnp.exp(jnp.where(band_ok, _dot_nt(q, keys) * scale + b_ref[j], NEG_INF) - wide(lse_q))
                    ds = p * (_dot_nt(dav, vals) - wide(dl_q))
                    dq = _dot_nn(ds.astype(BF16), keys)
                    db_ref[j] += ds
                    pn = jnp.exp(jnp.where(next_ok, _dot_nt(qn, kc) * scale + b_ref[j, :, :SPAN], NEG_INF) - get("lsen"))
                    dsn = pn * (_dot_nt(dan, vc) - get("dln"))
                    both = lambda cur_part, next_part: jnp.concatenate([cur_part.astype(BF16), next_part.astype(BF16)], axis=0)
                    dk = _dot_tn(both(ds[:, SPAN:], dsn), jnp.concatenate([q, qn], axis=0))
                    dv = _dot_tn(both(p[:, SPAN:], pn), jnp.concatenate([dav, dan], axis=0))
                dq_ref[j, rows, :] = dq * scale
                dk_ref[j, rows, :] = dk * scale
                dv_ref[j, rows, :] = dv

        _for_residues(r, per, residue)

    per_group = HEADS_PER_GROUP // hp
    band = (hp, SPAN, 2 * SPAN)
    in_specs = [_head_spec(r, nb, hp, kind, group, jj) for kind in kinds for jj in range(hp)]
    in_specs.append(pl.BlockSpec(band, lambda j, n: (group * per_group + j, 0, 0)))
    operands = [source.get(kind, proj) for kind in kinds for _ in range(hp)] + [bias]
    out = pl.BlockSpec((hp, r * SPAN, HEAD_DIM), lambda j, n: (j, n, 0))
    return pl.pallas_call(
        body, name=f"attn_bwd_g{group}", grid=(per_group, nb),
        in_specs=in_specs,
        out_specs=[out] * 3 + [pl.BlockSpec(band, lambda j, n: (j, 0, 0))],
        out_shape=[jax.ShapeDtypeStruct((HEADS_PER_GROUP, S, HEAD_DIM), F32)] * 3
        + [jax.ShapeDtypeStruct((HEADS_PER_GROUP, SPAN, 2 * SPAN), F32)],
        compiler_params=_params(("parallel", "arbitrary"), VMEM_LIMIT),
    )(*operands)


def _dproj_call(dqkv, tails):
    S = tails[0].shape[0]
    width = len(dqkv) * GROUP_WIDTH + sum(t.shape[1] for t in tails)

    def body(*refs):
        o_ref = refs[-1]
        col = 0
        for ref in refs[:len(dqkv)]:
            for j in range(HEADS_PER_GROUP):
                o_ref[:, col:col + HEAD_DIM] = ref[j].astype(BF16)
                col += HEAD_DIM
        for ref in refs[len(dqkv):-1]:
            o_ref[:, col:col + ref.shape[1]] = ref[...]
            col += ref.shape[1]

    heads = pl.BlockSpec((HEADS_PER_GROUP, ROW_TILE, HEAD_DIM), lambda i: (0, i, 0))
    return pl.pallas_call(
        body, name="dproj_assemble", grid=(S // ROW_TILE,),
        in_specs=[heads] * len(dqkv) + [_row_spec(t.shape[1]) for t in tails],
        out_specs=_row_spec(width), out_shape=jax.ShapeDtypeStruct((S, width), BF16),
        compiler_params=_params(("parallel",)),
    )(*dqkv, *tails)


def _taps(width):
    return [(k, (width - 1 - k) // SUBLANES, (width - 1 - k) % SUBLANES) for k in range(width)]


def _shifted(win, width, pad, up):
    total = win.shape[0]
    for b in range(SUBLANES):
        taps = [(k, a) for k, a, bb in _taps(width) if bb == b]
        if not taps:
            continue
        if up:
            rolled = win if b == 0 else pltpu.roll(win, total - b, axis=0)
        else:
            rolled = win if b == 0 else pltpu.roll(win, b, axis=0)
        for k, a in taps:
            start = SUBLANES * a if up else pad - SUBLANES * a
            yield k, rolled[start:start + TIME_BLOCK, :]


def _conv_block(win, w_ref, width, pad):
    acc = None
    for k, rows in _shifted(win, width, pad, up=False):
        term = w_ref[k:k + 1, :] * rows
        acc = term if acc is None else acc + term
    return acc


def _conv_transpose_block(win, w_ref, width, pad):
    acc = None
    for k, rows in _shifted(win, width, pad, up=True):
        term = w_ref[k:k + 1, :] * rows
        acc = term if acc is None else acc + term
    return acc


def _conv_weight_grad(win, dy, dw_ref, width, pad):
    for k, rows in _shifted(win, width, pad, up=False):
        dw_ref[k:k + 1, :] += jnp.sum(dy * rows, axis=0, keepdims=True)


def _time_loop(S, step):
    def it(tb, carry):
        step(pl.multiple_of(tb * TIME_BLOCK, TIME_BLOCK))
        return carry

    lax.fori_loop(0, S // TIME_BLOCK, it, 0)


def _conv_fwd_call(proj, col0, w, b):
    S = proj.shape[0]
    C = w.shape[1]
    nt = C // LANES
    v0, g0 = col0 // LANES, (col0 + C) // LANES

    def body(val_ref, gate_ref, w_ref, b_ref, o_ref, pad_ref):
        pad_ref[0:CONV_PAD, :] = jnp.zeros((CONV_PAD, LANES), F32)
        pad_ref[CONV_PAD:, :] = val_ref[...] * _sigmoid(gate_ref[...])

        def step(t0):
            win = pad_ref[pl.ds(t0, TIME_BLOCK + CONV_PAD), :]
            o_ref[pl.ds(t0, TIME_BLOCK), :] = _conv_block(win, w_ref, CONV_WIDTH, CONV_PAD) + b_ref[...]

        _time_loop(S, step)

    seq = lambda off: pl.BlockSpec((S, LANES), lambda i: (0, off + i))
    return pl.pallas_call(
        body, name="conv_module", grid=(nt,),
        in_specs=[seq(v0), seq(g0), pl.BlockSpec((CONV_WIDTH, LANES), lambda i: (0, i)), pl.BlockSpec((1, LANES), lambda i: (0, i))],
        out_specs=seq(0), out_shape=jax.ShapeDtypeStruct((S, C), F32),
        scratch_shapes=[pltpu.VMEM((S + CONV_PAD, LANES), F32)],
        compiler_params=_params(("parallel",)),
    )(proj, proj, w, b)


def _conv_bwd_call(proj, col0, w, dc1):
    S = proj.shape[0]
    C = w.shape[1]
    nt = C // LANES
    v0, g0 = col0 // LANES, (col0 + C) // LANES

    def body(val_ref, gate_ref, w_ref, dy_ref, dval_ref, dgate_ref, dw_ref, db_ref, xpad_ref, dpad_ref, dwacc_ref):
        xpad_ref[0:CONV_PAD, :] = jnp.zeros((CONV_PAD, LANES), F32)
        xpad_ref[CONV_PAD:, :] = val_ref[...] * _sigmoid(gate_ref[...])
        dpad_ref[0:S, :] = dy_ref[...]
        dpad_ref[S:, :] = jnp.zeros((CONV_PAD, LANES), F32)
        dwacc_ref[...] = jnp.zeros_like(dwacc_ref)

        def step(t0):
            rows = pl.ds(t0, TIME_BLOCK)
            _conv_weight_grad(xpad_ref[pl.ds(t0, TIME_BLOCK + CONV_PAD), :], dy_ref[rows, :], dwacc_ref, CONV_WIDTH, CONV_PAD)
            dc0 = _conv_transpose_block(dpad_ref[pl.ds(t0, TIME_BLOCK + CONV_PAD), :], w_ref, CONV_WIDTH, CONV_PAD)
            sg = _sigmoid(gate_ref[rows, :])
            dval_ref[rows, :] = (dc0 * sg).astype(BF16)
            dgate_ref[rows, :] = (dc0 * val_ref[rows, :] * sg * (1.0 - sg)).astype(BF16)

        _time_loop(S, step)
        dw_ref[...] = dwacc_ref[...]
        db_ref[...] = jnp.sum(dy_ref[...], axis=0, keepdims=True)

    seq = lambda off: pl.BlockSpec((S, LANES), lambda i: (0, off + i))
    return pl.pallas_call(
        body, name="conv_module_bwd", grid=(nt,),
        in_specs=[seq(v0), seq(g0), pl.BlockSpec((CONV_WIDTH, LANES), lambda i: (0, i)), seq(0)],
        out_specs=[seq(0), seq(0), pl.BlockSpec((CONV_PAD, LANES), lambda i: (0, i)), pl.BlockSpec((1, LANES), lambda i: (0, i))],
        out_shape=[jax.ShapeDtypeStruct((S, C), BF16), jax.ShapeDtypeStruct((S, C), BF16),
                   jax.ShapeDtypeStruct((CONV_PAD, C), F32), jax.ShapeDtypeStruct((1, C), F32)],
        scratch_shapes=[pltpu.VMEM((S + CONV_PAD, LANES), F32), pltpu.VMEM((S + CONV_PAD, LANES), F32),
                        pltpu.VMEM((CONV_PAD, LANES), F32)],
        compiler_params=_params(("parallel",)),
    )(proj, proj, w, dc1)


def _ffn_fwd_call(u, w, b):
    S, C2 = u.shape
    C = C2 // 2
    nt = C // LANES

    def body(ug_ref, uv_ref, wg_ref, wv_ref, bg_ref, bv_ref, f_ref, pg_ref, pv_ref):
        zeros = jnp.zeros((FFN_PAD, LANES), F32)
        pg_ref[0:FFN_PAD, :] = zeros
        pv_ref[0:FFN_PAD, :] = zeros
        pg_ref[FFN_PAD:, :] = ug_ref[...]
        pv_ref[FFN_PAD:, :] = uv_ref[...]

        def step(t0):
            win = pl.ds(t0, TIME_BLOCK + FFN_PAD)
            cg = _conv_block(pg_ref[win, :], wg_ref, FFN_CONV_WIDTH, FFN_PAD) + bg_ref[...]
            cv = _conv_block(pv_ref[win, :], wv_ref, FFN_CONV_WIDTH, FFN_PAD) + bv_ref[...]
            f_ref[pl.ds(t0, TIME_BLOCK), :] = (_gelu(cg) * cv).astype(BF16)

        _time_loop(S, step)

    seq = lambda off: pl.BlockSpec((S, LANES), lambda i: (0, off + i))
    wsp = lambda off: pl.BlockSpec((FFN_CONV_WIDTH, LANES), lambda i: (0, off + i))
    bsp = lambda off: pl.BlockSpec((1, LANES), lambda i: (0, off + i))
    return pl.pallas_call(
        body, name="ffn_conv_geglu", grid=(nt,),
        in_specs=[seq(0), seq(nt), wsp(0), wsp(nt), bsp(0), bsp(nt)],
        out_specs=seq(0), out_shape=jax.ShapeDtypeStruct((S, C), BF16),
        scratch_shapes=[pltpu.VMEM((S + FFN_PAD, LANES), F32)] * 2,
        compiler_params=_params(("parallel",)),
    )(u, u, w, w, b, b)


def _ffn_bwd_call(u, w, b, df):
    S, C2 = u.shape
    C = C2 // 2
    nt = C // LANES

    def body(ug_ref, uv_ref, wg_ref, wv_ref, bg_ref, bv_ref, df_ref,
             du_ref, dwg_ref, dwv_ref, dbg_ref, dbv_ref,
             pg_ref, pv_ref, dg_ref, dv_ref, dwg_acc, dwv_acc, dbg_acc, dbv_acc):
        zeros = jnp.zeros((FFN_PAD, LANES), F32)
        pg_ref[0:FFN_PAD, :] = zeros
        pv_ref[0:FFN_PAD, :] = zeros
        pg_ref[FFN_PAD:, :] = ug_ref[...]
        pv_ref[FFN_PAD:, :] = uv_ref[...]
        dg_ref[S:, :] = zeros
        dv_ref[S:, :] = zeros
        dwg_acc[...] = jnp.zeros_like(dwg_acc)
        dwv_acc[...] = jnp.zeros_like(dwv_acc)
        dbg_acc[...] = jnp.zeros_like(dbg_acc)
        dbv_acc[...] = jnp.zeros_like(dbv_acc)

        def first(t0):
            win = pl.ds(t0, TIME_BLOCK + FFN_PAD)
            rows = pl.ds(t0, TIME_BLOCK)
            xg = pg_ref[win, :]
            xv = pv_ref[win, :]
            cg = _conv_block(xg, wg_ref, FFN_CONV_WIDTH, FFN_PAD) + bg_ref[...]
            cv = _conv_block(xv, wv_ref, FFN_CONV_WIDTH, FFN_PAD) + bv_ref[...]
            dfb = df_ref[rows, :]
            gelu, gelu_grad = _gelu_and_grad(cg)
            dcg = dfb * cv * gelu_grad
            dcv = dfb * gelu
            dg_ref[rows, :] = dcg
            dv_ref[rows, :] = dcv
            _conv_weight_grad(xg, dcg, dwg_acc, FFN_CONV_WIDTH, FFN_PAD)
            _conv_weight_grad(xv, dcv, dwv_acc, FFN_CONV_WIDTH, FFN_PAD)
            dbg_acc[...] += jnp.sum(dcg, axis=0, keepdims=True)
            dbv_acc[...] += jnp.sum(dcv, axis=0, keepdims=True)

        def second(t0):
            win = pl.ds(t0, TIME_BLOCK + FFN_PAD)
            rows = pl.ds(t0, TIME_BLOCK)
            du_ref[0, rows, :] = _conv_transpose_block(dg_ref[win, :], wg_ref, FFN_CONV_WIDTH, FFN_PAD).astype(BF16)
            du_ref[1, rows, :] = _conv_transpose_block(dv_ref[win, :], wv_ref, FFN_CONV_WIDTH, FFN_PAD).astype(BF16)

        _time_loop(S, first)
        _time_loop(S, second)
        dwg_ref[...] = dwg_acc[...]
        dwv_ref[...] = dwv_acc[...]
        dbg_ref[...] = dbg_acc[...]
        dbv_ref[...] = dbv_acc[...]

    seq = lambda off: pl.BlockSpec((S, LANES), lambda i: (0, off + i))
    wsp = lambda off: pl.BlockSpec((FFN_CONV_WIDTH, LANES), lambda i: (0, off + i))
    bsp = lambda off: pl.BlockSpec((1, LANES), lambda i: (0, off + i))
    return pl.pallas_call(
        body, name="ffn_conv_geglu_bwd", grid=(nt,),
        in_specs=[seq(0), seq(nt), wsp(0), wsp(nt), bsp(0), bsp(nt), seq(0)],
        out_specs=[pl.BlockSpec((2, S, LANES), lambda i: (0, 0, i)),
                   pl.BlockSpec((SUBLANES, LANES), lambda i: (0, i)), pl.BlockSpec((SUBLANES, LANES), lambda i: (0, i)),
                   bsp(0), bsp(0)],
        out_shape=[jax.ShapeDtypeStruct((2, S, C), BF16)] + [jax.ShapeDtypeStruct((SUBLANES, C), F32)] * 2
        + [jax.ShapeDtypeStruct((1, C), F32)] * 2,
        scratch_shapes=[pltpu.VMEM((S + FFN_PAD, LANES), F32)] * 4 + [pltpu.VMEM((SUBLANES, LANES), F32)] * 2
        + [pltpu.VMEM((1, LANES), F32)] * 2,
        compiler_params=_params(("parallel",)),
    )(u, u, w, w, b, b, df)


def _adamw(w_ref, g_ref, m_ref, v_ref, d_ref, mo_ref, vo_ref):
    gv = g_ref[...]
    mn = ADAM_B1 * m_ref[...] + (1.0 - ADAM_B1) * gv
    vn = ADAM_B2 * v_ref[...] + (1.0 - ADAM_B2) * (gv * gv)
    mo_ref[...] = mn
    vo_ref[...] = vn
    m_hat = mn * (1.0 / (1.0 - ADAM_B1 ** ADAM_STEP))
    v_hat = vn * (1.0 / (1.0 - ADAM_B2 ** ADAM_STEP))
    d_ref[...] = -ADAM_LR * (m_hat / (jnp.sqrt(v_hat) + ADAM_EPS) + ADAM_WD * w_ref[...])


def _adamw_call(w, g, m, v, name):
    R, C = w.shape
    tr = _row_tile(R, C)
    spec = pl.BlockSpec((tr, C), lambda i: (i, 0))
    return pl.pallas_call(
        _adamw_body(), name=name, grid=(R // tr,),
        in_specs=[spec] * 4, out_specs=[spec] * 3,
        out_shape=[jax.ShapeDtypeStruct((R, C), F32)] * 3,
        compiler_params=_params(("parallel",)),
    )(w, g, m, v)


def _adamw_body():
    def body(*refs):
        _adamw(*refs)

    return body


def _adamw_small_call(ws, gs, ms, vs):
    n = len(ws)

    def body(*refs):
        w_refs, g_refs, m_refs, v_refs, d_refs, mo_refs, vo_refs = (refs[i * n:(i + 1) * n] for i in range(7))
        for i in range(n):
            _adamw(w_refs[i], g_refs[i], m_refs[i], v_refs[i], d_refs[i], mo_refs[i], vo_refs[i])

    whole = pl.BlockSpec(memory_space=pltpu.VMEM)
    outs = pl.pallas_call(
        body, name="adamw_small",
        in_specs=[whole] * (4 * n), out_specs=[whole] * (3 * n),
        out_shape=[jax.ShapeDtypeStruct(w.shape, F32) for w in ws] * 3,
    )(*ws, *gs, *ms, *vs)
    return outs[:n], outs[n:2 * n], outs[2 * n:]


def _position():
    return lax.axis_index("x"), lax.axis_index("y"), lax.axis_index("c")


def _chip_peers(x, y):
    return [(x, 1 - y), (1 - x, y), (1 - x, 1 - y)]


def _half_rows(ref, core, rows):
    h = rows // 2
    start = pl.multiple_of(core * h, 16)
    return ref.at[pl.ds(start, h), :] if len(ref.shape) == 2 else ref.at[:, pl.ds(start, h), :]


def _shard_half(ref, shard, core, rows):
    h = rows // 2
    return ref.at[shard, pl.ds(pl.multiple_of(core * h, 16), h), :]


ANY = pl.BlockSpec(memory_space=pl.ANY)


def _allgather_call(shards, whole):
    n, nw = len(shards), len(whole)
    outs_shape = [jax.ShapeDtypeStruct((N_CHIPS,) + s.shape, s.dtype) for s in shards + whole]

    def body(*refs):
        ins, outs = refs[:n + nw], refs[n + nw:2 * (n + nw)]
        send_sems, recv_sems, pass_send, pass_recv, own_send, own_recv = refs[2 * (n + nw):]
        x, y, c = _position()
        chip = 2 * x + y
        peers = _chip_peers(x, y)
        sent, local = [], []
        for i in range(n + nw):
            cp = pltpu.make_async_remote_copy(src_ref=ins[i], dst_ref=outs[i].at[chip], send_sem=own_send.at[i],
                                              recv_sem=own_recv.at[i], device_id=(x, y, 1 - c), device_id_type=MESH)
            cp.start()
            local.append(cp)
            rows = ins[i].shape[0]
            for k, (px, py) in enumerate(peers):
                if i < n:
                    src, dst = _half_rows(ins[i], c, rows), _shard_half(outs[i], chip, c, rows)
                else:
                    src, dst = ins[i], outs[i].at[chip]
                cp = pltpu.make_async_remote_copy(src_ref=src, dst_ref=dst, send_sem=send_sems.at[i, k],
                                                  recv_sem=recv_sems.at[i, k], device_id=(px, py, c), device_id_type=MESH)
                cp.start()
                sent.append(cp)
        passed = []
        for i in range(n + nw):
            rows = ins[i].shape[0]
            for k, (px, py) in enumerate(peers):
                landed = _shard_half(outs[i], 2 * px + py, c, rows) if i < n else outs[i].at[2 * px + py]
                pltpu.make_async_remote_copy(src_ref=landed, dst_ref=landed, send_sem=send_sems.at[i, k],
                                             recv_sem=recv_sems.at[i, k], device_id=(px, py, c), device_id_type=MESH).wait_recv()
                if i < n:
                    cp = pltpu.make_async_remote_copy(src_ref=landed, dst_ref=landed, send_sem=pass_send.at[i, k],
                                                      recv_sem=pass_recv.at[i, k], device_id=(x, y, 1 - c), device_id_type=MESH)
                    cp.start()
                    passed.append(cp)
        for cp in sent:
            cp.wait_send()
        for cp in passed:
            cp.wait()
        for cp in local:
            cp.wait()

    return pl.pallas_call(
        body, name="weight_allgather",
        in_specs=[ANY] * (n + nw), out_specs=[ANY] * (n + nw), out_shape=outs_shape,
        scratch_shapes=[pltpu.SemaphoreType.DMA((n + nw, 3)), pltpu.SemaphoreType.DMA((n + nw, 3)),
                        pltpu.SemaphoreType.DMA((n, 3)), pltpu.SemaphoreType.DMA((n, 3)),
                        pltpu.SemaphoreType.DMA((n + nw,)), pltpu.SemaphoreType.DMA((n + nw,))],
    )(*shards, *whole)


HBM_SPEC = pl.BlockSpec(memory_space=pltpu.HBM)
SEM_SPEC = pl.BlockSpec(memory_space=pltpu.SEMAPHORE)
DATAFLOW = pltpu.SideEffectType.DATAFLOW_SIDE_EFFECTING


def _in_hbm(a):
    return pltpu.with_memory_space_constraint(a, pltpu.HBM)


def _split_start(name, groups, after):
    spans, arrays = [], []
    for srcs, lands, _, _ in groups:
        spans.append((len(arrays), len(srcs), len(lands)))
        arrays += list(srcs) + list(lands)
    na, ng = len(arrays), len(groups)

    def body(*refs):
        sems, token = refs[na + 1:na + 1 + 2 * ng], refs[-1]
        for g, (_, _, _, copies) in enumerate(groups):
            off, ns, nl = spans[g]
            for src, dst, dev, idx in copies(refs[off:off + ns], refs[off + ns:off + ns + nl]):
                pltpu.make_async_remote_copy(src_ref=src, dst_ref=dst, send_sem=sems[2 * g].at[idx], recv_sem=sems[2 * g + 1].at[idx],
                                             device_id=dev, device_id_type=MESH).start()
        token[...] = jnp.zeros_like(token)

    outs = pl.pallas_call(
        body, name=name,
        in_specs=[HBM_SPEC] * na + [ANY],
        out_specs=[SEM_SPEC] * (2 * ng) + [HBM_SPEC] * na + [pl.BlockSpec(memory_space=pltpu.VMEM)],
        out_shape=[pltpu.SemaphoreType.DMA((n_sems,)) for _, _, n_sems, _ in groups for _ in range(2)]
        + [pltpu.HBM(a.shape, a.dtype) for a in arrays] + [jax.ShapeDtypeStruct((SUBLANES, LANES), F32)],
        input_output_aliases={i: 2 * ng + i for i in range(na)},
        compiler_params=pltpu.CompilerParams(has_side_effects=DATAFLOW),
    )(*[_in_hbm(a) for a in arrays], after)
    started = []
    for g, (off, ns, nl) in enumerate(spans):
        thru = outs[2 * ng + off:2 * ng + off + ns + nl]
        started.append(dict(send=outs[2 * g], recv=outs[2 * g + 1], srcs=list(thru[:ns]), lands=list(thru[ns:]),
                            tile=outs[-1], token=outs[-1][0, 0]))
    return started


def _split_wait(name, started, copies, after):
    n, m = len(started["srcs"]), len(started["lands"])

    def body(*refs):
        src_refs, land_refs = refs[:n], refs[n:n + m]
        send_sem, recv_sem = refs[n + m], refs[n + m + 1]
        for src, dst, dev, idx in copies(src_refs, land_refs):
            cp = pltpu.make_async_remote_copy(src_ref=src, dst_ref=dst, send_sem=send_sem.at[idx], recv_sem=recv_sem.at[idx],
                                              device_id=dev, device_id_type=MESH)
            cp.wait_send()
            cp.wait_recv()

    arrays = started["srcs"] + started["lands"]
    outs = pl.pallas_call(
        body, name=name,
        in_specs=[HBM_SPEC] * (n + m) + [SEM_SPEC, SEM_SPEC, ANY],
        out_specs=[HBM_SPEC] * (n + m),
        out_shape=[pltpu.HBM(a.shape, a.dtype) for a in arrays],
        input_output_aliases={i: i for i in range(n + m)},
        compiler_params=pltpu.CompilerParams(has_side_effects=DATAFLOW),
    )(*arrays, started["send"], started["recv"], after)
    return list(outs)


def _gather_copies(srcs, lands):
    x, y, c = _position()
    chip = 2 * x + y
    targets = [(px, py, c) for px, py in _chip_peers(x, y)] + [(x, y, 1 - c)]
    return [(s, l.at[chip], dev, len(targets) * i + k) for i, (s, l) in enumerate(zip(srcs, lands)) for k, dev in enumerate(targets)]


def _sibling_copies(srcs, lands):
    x, y, c = _position()
    return [(_half_rows(srcs[0], 1 - c, srcs[0].shape[1]), lands[0], (x, y, 1 - c), 0)]


def _exchange_copies(srcs, lands):
    x, y, c = _position()
    return [(srcs[0].at[2 * px + py], lands[0].at[k], (px, py, c), k) for k, (px, py) in enumerate(_chip_peers(x, y))]


def _pair_sum_call(grad, recv, core, name):
    _, h, B = recv.shape
    tr = _row_tile(h, B)

    def body(core_ref, g_ref, r_ref, o_ref, ob_ref):
        s = g_ref[...] + r_ref[...]
        o_ref[...] = s
        ob_ref[...] = s.astype(BF16)

    g_spec = pl.BlockSpec((None, tr, B), lambda q, i, core_ref: (q, core_ref[0] * (h // tr) + i, 0))
    spec = pl.BlockSpec((None, tr, B), lambda q, i, core_ref: (q, i, 0))
    return pl.pallas_call(
        body, name=name,
        grid_spec=pltpu.PrefetchScalarGridSpec(num_scalar_prefetch=1, grid=(N_CHIPS, h // tr), in_specs=[g_spec, spec],
                                               out_specs=[spec, spec]),
        out_shape=[jax.ShapeDtypeStruct(recv.shape, F32), jax.ShapeDtypeStruct(recv.shape, BF16)],
        compiler_params=_params(("parallel", "parallel")),
    )(core, grad, recv)


def _chip_sum_call(partial, recv, chip_core, name):
    _, h, B = recv.shape
    tr = _row_tile(h, B)

    def body(cc_ref, p_ref, r_ref, o_ref):
        o_ref[...] = ((p_ref[...] + r_ref[0].astype(F32)) + r_ref[1].astype(F32)) + r_ref[2].astype(F32)

    return pl.pallas_call(
        body, name=name,
        grid_spec=pltpu.PrefetchScalarGridSpec(
            num_scalar_prefetch=1, grid=(h // tr,),
            in_specs=[pl.BlockSpec((None, tr, B), lambda i, cc_ref: (cc_ref[0], i, 0)),
                      pl.BlockSpec((3, tr, B), lambda i, cc_ref: (0, i, 0))],
            out_specs=pl.BlockSpec((tr, B), lambda i, cc_ref: (cc_ref[1] * (h // tr) + i, 0))),
        out_shape=jax.ShapeDtypeStruct((2 * h, B), F32),
        compiler_params=_params(("parallel",)),
    )(chip_core, partial, recv)


def _sibling_assemble_call(shards, name="grad_sibling_assemble"):
    n = len(shards)

    def body(*refs):
        ins, outs = refs[:n], refs[n:2 * n]
        send_sems, recv_sems = refs[2 * n:]
        x, y, c = _position()
        copies = []
        for i in range(n):
            rows = shards[i].shape[0]
            cp = pltpu.make_async_remote_copy(src_ref=_half_rows(ins[i], c, rows), dst_ref=_half_rows(outs[i], c, rows),
                                              send_sem=send_sems.at[i], recv_sem=recv_sems.at[i],
                                              device_id=(x, y, 1 - c), device_id_type=MESH)
            cp.start()
            copies.append(cp)
        for cp in copies:
            cp.wait()

    return pl.pallas_call(
        body, name=name,
        in_specs=[ANY] * n, out_specs=[ANY] * n,
        out_shape=[jax.ShapeDtypeStruct(s.shape, F32) for s in shards],
        input_output_aliases={i: i for i in range(n)},
        scratch_shapes=[pltpu.SemaphoreType.DMA((n,)), pltpu.SemaphoreType.DMA((n,))],
    )(*shards)


N_DEVICES = 8


def _allsum_copies(srcs, lands):
    x, y, c = _position()
    me = 4 * x + 2 * y + c
    out = []
    for k in range(1, N_DEVICES):
        peer = (1 - x if k & 4 else x, 1 - y if k & 2 else y, 1 - c if k & 1 else c)
        out.append((srcs[0], lands[0].at[me], peer, k - 1))
    return out


def _ordered_sum_call(mine, landed, me):
    rows = mine.shape[0]

    def body(me_ref, x_ref, l_ref, o_ref):
        acc = jnp.where(me_ref[0] == 0, x_ref[...], l_ref[0])
        for d in range(1, N_DEVICES):
            acc = acc + jnp.where(me_ref[0] == d, x_ref[...], l_ref[d])
        o_ref[...] = acc

    return pl.pallas_call(
        body, name="small_grad_sum",
        in_specs=[pl.BlockSpec(memory_space=pltpu.SMEM), pl.BlockSpec(memory_space=pltpu.VMEM), pl.BlockSpec(memory_space=pltpu.VMEM)],
        out_specs=pl.BlockSpec(memory_space=pltpu.VMEM),
        out_shape=jax.ShapeDtypeStruct((rows, LANES), F32),
    )(me, mine, landed)


def _pack(arrays):
    flat = jnp.concatenate([a.reshape(-1).astype(F32) for a in arrays])
    rows = -(-flat.shape[0] // LANES)
    rows = -(-rows // SUBLANES) * SUBLANES
    flat = jnp.pad(flat, (0, rows * LANES - flat.shape[0]))
    return flat.reshape(rows, LANES)


def _unpack(packed, shapes):
    flat = packed.reshape(-1)
    out, off = [], 0
    for shp in shapes:
        size = int(np.prod(shp))
        out.append(flat[off:off + size].reshape(shp))
        off += size
    return out


def _local_step(xs, target, P, late_weights, on_grad):
    S, D = xs.shape
    qkv_width = 3 * N_HEADS * HEAD_DIM
    glu_col0, gate_col0 = qkv_width, qkv_width + 2 * D
    shard_major = lambda g: g.reshape(N_CHIPS, g.shape[0] // N_CHIPS, g.shape[1])

    h1 = _rms_fwd_call(xs, P["norm_mix_pre"])
    proj = _matmul(h1, P["w_in"], "nn", "proj_in")
    buckets = _bucket_tables()
    bias = _bias_table_call(P["rel_bias"], buckets)
    parts = []
    for g in range(N_GROUPS):
        parts += _attn_fwd_call(proj, bias, g)
    a, a_bf, lse = _attn_merge_call(parts)
    P = dict(P, **late_weights("mix", a_bf))
    y_a = _matmul(a_bf, P["w_attn_out"], "nn", "attn_out")
    c1 = _conv_fwd_call(proj, glu_col0, P["conv_dw_w"], P["conv_dw_b"])
    cact = _ln_silu_call(c1, P["conv_ln_g"], P["conv_ln_b"])
    y_c = _matmul(cact, P["conv_pw_w"], "nn", "conv_pw")
    mixed = _mix_call(proj, gate_col0, P["b_gate"], y_a, y_c)
    out = _matmul(mixed, P["w_out"], "nn", "mix_out")
    x1, h2 = _res1_call(xs, out, P["norm_mix_post"], P["norm_ffn_pre"])
    P = dict(P, **late_weights("ffn", h2))
    u = _matmul(h2, P["w_up"], "nn", "ffn_up")
    f = _ffn_fwd_call(u, P["ffn_conv_w"], P["ffn_conv_b"])
    yff = _matmul(f, P["w_down"], "nn", "ffn_down")
    loss_tile, dx2 = _loss_call(yff, x1, P["norm_ffn_post"], target)

    G = {}
    dyff, G["norm_ffn_post"] = _rms_bwd_call(yff, P["norm_ffn_post"], dx2, "rms_ffn_post_bwd")
    zero = on_grad("w_down", shard_major(_matmul(f, dyff, "tn", "ffn_down_dw")))
    df = _matmul(dyff, P["w_down"], "nt", "ffn_down_dx")
    du, dwg, dwv, dbg, dbv = _ffn_bwd_call(u, P["ffn_conv_w"], P["ffn_conv_b"] + zero, df)
    G["ffn_conv_w"] = jnp.concatenate([dwg[:FFN_CONV_WIDTH], dwv[:FFN_CONV_WIDTH]], axis=1)
    G["ffn_conv_b"] = jnp.concatenate([dbg, dbv], axis=1)
    zero = on_grad("w_up", _matmul(h2, du, "tn", "ffn_up_dw", out_shards=True))
    dh2 = _matmul(du, P["w_up"], "nt", "ffn_up_dx")
    dx1, dout, G["norm_ffn_pre"], G["norm_mix_post"] = _mid_bwd_call(x1, P["norm_ffn_pre"] + zero, dh2, dx2, out, P["norm_mix_post"])
    zero = on_grad("w_out", shard_major(_matmul(mixed, dout, "tn", "mix_out_dw")))
    dmixed = _matmul(dout, P["w_out"], "nt", "mix_out_dx")
    dya, dyc, dga, dgc, dba, dbc = _mix_bwd_call(dmixed, proj, gate_col0, P["b_gate"] + zero, y_a, y_c)
    G["b_gate"] = jnp.concatenate([dba, dbc], axis=1)
    zero = on_grad("w_attn_out", _matmul(a_bf, dya, "tn", "attn_out_dw", out_shards=True))
    zero = zero + on_grad("conv_pw_w", shard_major(_matmul(cact, dyc, "tn", "conv_pw_dw")))
    da = _matmul(dya, P["w_attn_out"], "nt", "attn_out_dx")
    dcact = _matmul(dyc, P["conv_pw_w"], "nt", "conv_pw_dx")
    dc1, G["conv_ln_g"], G["conv_ln_b"] = _ln_silu_bwd_call(c1, P["conv_ln_g"] + zero, P["conv_ln_b"], dcact)
    dval, dgate, dw_dw, G["conv_dw_b"] = _conv_bwd_call(proj, glu_col0, P["conv_dw_w"], dc1)
    G["conv_dw_w"] = dw_dw[:CONV_WIDTH]
    delta = _attn_delta_call(a, da)
    dqs, dks, dvs, dbs = [], [], [], []
    for g in range(N_GROUPS):
        dq, dk, dv, db = _attn_bwd_call(proj, bias, da, lse, delta, g)
        dqs.append(dq)
        dks.append(dk)
        dvs.append(dv)
        dbs.append(db)
    G["rel_bias"] = _bias_grad_call(jnp.concatenate(dbs, axis=0), buckets)
    dproj = _dproj_call(dqs + dks + dvs, [dval, dgate, dga, dgc])
    zero = on_grad("w_in", _matmul(h1, dproj, "tn", "proj_in_dw", out_shards=True, tm=512))
    dh1 = _matmul(dproj, P["w_in"], "nt", "proj_in_dx")
    grad_x, G["norm_mix_pre"] = _in_bwd_call(xs, P["norm_mix_pre"] + zero, dh1, dx1)
    return loss_tile, grad_x, G


def kernel(x, w_in, b_gate, rel_bias, w_attn_out, conv_dw_w, conv_dw_b, conv_ln_g, conv_ln_b, conv_pw_w, w_out, norm_mix_pre, norm_mix_post, norm_ffn_pre, norm_ffn_post, w_up, ffn_conv_w, ffn_conv_b, w_down, loss_target, m_w_in, m_b_gate, m_rel_bias, m_w_attn_out, m_conv_dw_w, m_conv_dw_b, m_conv_ln_g, m_conv_ln_b, m_conv_pw_w, m_w_out, m_norm_mix_pre, m_norm_mix_post, m_norm_ffn_pre, m_norm_ffn_post, m_w_up, m_ffn_conv_w, m_ffn_conv_b, m_w_down, v_w_in, v_b_gate, v_rel_bias, v_w_attn_out, v_conv_dw_w, v_conv_dw_b, v_conv_ln_g, v_conv_ln_b, v_conv_pw_w, v_w_out, v_norm_mix_pre, v_norm_mix_post, v_norm_ffn_pre, v_norm_ffn_post, v_w_up, v_ffn_conv_w, v_ffn_conv_b, v_w_down):
    weights = dict(w_in=w_in, b_gate=b_gate, rel_bias=rel_bias, w_attn_out=w_attn_out, conv_dw_w=conv_dw_w, conv_dw_b=conv_dw_b,
                   conv_ln_g=conv_ln_g, conv_ln_b=conv_ln_b, conv_pw_w=conv_pw_w, w_out=w_out, norm_mix_pre=norm_mix_pre,
                   norm_mix_post=norm_mix_post, norm_ffn_pre=norm_ffn_pre, norm_ffn_post=norm_ffn_post, w_up=w_up,
                   ffn_conv_w=ffn_conv_w, ffn_conv_b=ffn_conv_b, w_down=w_down)
    m_in = dict(w_in=m_w_in, b_gate=m_b_gate, rel_bias=m_rel_bias, w_attn_out=m_w_attn_out, conv_dw_w=m_conv_dw_w,
                conv_dw_b=m_conv_dw_b, conv_ln_g=m_conv_ln_g, conv_ln_b=m_conv_ln_b, conv_pw_w=m_conv_pw_w, w_out=m_w_out,
                norm_mix_pre=m_norm_mix_pre, norm_mix_post=m_norm_mix_post, norm_ffn_pre=m_norm_ffn_pre,
                norm_ffn_post=m_norm_ffn_post, w_up=m_w_up, ffn_conv_w=m_ffn_conv_w, ffn_conv_b=m_ffn_conv_b, w_down=m_w_down)
    v_in = dict(w_in=v_w_in, b_gate=v_b_gate, rel_bias=v_rel_bias, w_attn_out=v_w_attn_out, conv_dw_w=v_conv_dw_w,
                conv_dw_b=v_conv_dw_b, conv_ln_g=v_conv_ln_g, conv_ln_b=v_conv_ln_b, conv_pw_w=v_conv_pw_w, w_out=v_w_out,
                norm_mix_pre=v_norm_mix_pre, norm_mix_post=v_norm_mix_post, norm_ffn_pre=v_norm_ffn_pre,
                norm_ffn_post=v_norm_ffn_post, w_up=v_w_up, ffn_conv_w=v_ffn_conv_w, ffn_conv_b=v_ffn_conv_b, w_down=v_w_down)
    names = list(weights)
    xi, yi, ci = _position()
    chip = 2 * xi + yi
    core_arr = jnp.reshape(ci, (1,)).astype(jnp.int32)

    xs = x[0]
    target = loss_target[0]
    S, D = xs.shape

    big = ["w_in", "w_attn_out", "conv_pw_w", "w_out", "w_up", "w_down"]
    row_sharded = ("conv_pw_w", "w_out", "w_down")
    bf16_shard = {k: weights[k][0].astype(BF16) for k in big}
    natural = lambda k, g: g.reshape(-1, g.shape[2]) if k in row_sharded else g
    w_in_full, dw4, fc4 = _allgather_call([bf16_shard["w_in"]], [conv_dw_w[0], ffn_conv_w[0]])
    late_sets = dict(mix=["w_attn_out", "conv_pw_w", "w_out"], ffn=["w_up", "w_down"])
    late_groups = []
    for keys in late_sets.values():
        srcs = [bf16_shard[k] for k in keys]
        late_groups.append((srcs, [lax.empty((N_CHIPS,) + s.shape, BF16) for s in srcs], 4 * len(keys), _gather_copies))
    started = dict(zip(late_sets, _split_start("gather_late_start", late_groups, w_in_full)))
    launched = started["mix"]["token"]

    def late_weights(tag, after):
        landed = _split_wait(f"gather_{tag}_wait", started[tag], _gather_copies, after)[len(late_sets[tag]):]
        return {k: natural(k, g) for k, g in zip(late_sets[tag], landed)}

    chip_core = jnp.stack([chip, ci]).astype(jnp.int32)
    exchanging, pending = {}, {}

    def launch(tag, g3, after):
        keys, groups, partial = [], [], {}
        for k in list(exchanging):
            gk, r1 = _split_wait(f"sibling_exchange_wait_{k}", exchanging.pop(k), _sibling_copies, after)
            partial[k], s16 = _pair_sum_call(gk, r1, core_arr, f"pair_sum_{k}")
            keys.append(k)
            groups.append(([s16], [lax.empty((3,) + s16.shape[1:], BF16)], 3, _exchange_copies))
        if g3 is not None:
            groups.append(([g3], [lax.empty((N_CHIPS, g3.shape[1] // 2, g3.shape[2]), F32)], 1, _sibling_copies))
        begun = _split_start(f"grad_exchange_start_{tag}", groups, core_arr)
        for k, st in zip(keys, begun):
            pending[k] = (partial[k], st)
        if g3 is not None:
            exchanging[tag] = begun[-1]
        return begun[0]["token"]

    def on_grad(k, g3):
        return launch(k, g3, g3[0, :SUBLANES, :LANES])

    def pair_up(after):
        return launch("last", None, after)

    def finish(keys, after, tag):
        halves = []
        for k in keys:
            s32, st = pending[k]
            recv2 = _split_wait(f"chip_exchange_wait_{k}", st, _exchange_copies, after)[1]
            halves.append(_chip_sum_call(s32, recv2, chip_core, f"chip_sum_{k}"))
        return dict(zip(keys, _sibling_assemble_call(halves, f"grad_sibling_assemble_{tag}")))

    P = dict(w_in=w_in_full, conv_dw_w=jnp.concatenate(list(dw4), axis=1), ffn_conv_w=jnp.concatenate(list(fc4), axis=1),
             b_gate=b_gate, rel_bias=rel_bias, conv_dw_b=conv_dw_b, conv_ln_g=conv_ln_g, conv_ln_b=conv_ln_b,
             norm_mix_pre=norm_mix_pre + launched, norm_mix_post=norm_mix_post, norm_ffn_pre=norm_ffn_pre,
             norm_ffn_post=norm_ffn_post, ffn_conv_b=ffn_conv_b)
    loss_tile, grad_x, G = _local_step(xs, target, P, late_weights, on_grad)

    small = [k for k in names if k not in big]
    packed = _pack([loss_tile[:1]] + [G[k] for k in small])
    (allsum,) = _split_start("small_grad_allsum_start",
                             [([packed], [jnp.zeros((N_DEVICES,) + packed.shape, F32)], N_DEVICES - 1, _allsum_copies)], core_arr)
    launched = pair_up(allsum["tile"] + grad_x[:SUBLANES, :LANES])

    reduced, grads, deltas, new_m, new_v = {}, {}, {}, {}, {}

    def update(keys):
        for k in keys:
            d, mn, vn = _adamw_call(weights[k][0], reduced[k], m_in[k][0], v_in[k][0], f"adamw_{k}")
            grads[k], deltas[k], new_m[k], new_v[k] = reduced[k][None], d[None], mn[None], vn[None]

    others = [k for k in big if k != "w_in"]
    reduced.update(finish(others, allsum["tile"] + launched, "others"))
    update(others)

    me = jnp.reshape(4 * xi + 2 * yi + ci, (1,)).astype(jnp.int32)
    mine, landed = _split_wait("small_grad_allsum_wait", allsum, _allsum_copies, deltas["w_up"])
    summed_block = _ordered_sum_call(mine, landed, me)
    loss_row, *summed = _unpack(summed_block, [(1, LANES)] + [G[k].shape for k in small])
    loss = loss_row[0, 0]
    for k, gsum in zip(small, summed):
        if k in ("conv_dw_w", "ffn_conv_w"):
            cols = weights[k].shape[2]
            reduced[k] = lax.dynamic_slice_in_dim(gsum, chip * cols, cols, axis=1)
        else:
            reduced[k] = gsum
    flat2 = lambda t: t.reshape(-1, t.shape[-1]) if t.ndim == 3 else t
    ds, mns, vns = _adamw_small_call([flat2(weights[k]) for k in small], [reduced[k] for k in small],
                                     [flat2(m_in[k]) for k in small], [flat2(v_in[k]) for k in small])
    for k, dk_, mk, vk in zip(small, ds, mns, vns):
        shape = weights[k].shape
        grads[k], deltas[k], new_m[k], new_v[k] = reduced[k].reshape(shape), dk_.reshape(shape), mk.reshape(shape), vk.reshape(shape)

    reduced.update(finish(["w_in"], deltas["norm_mix_pre"], "w_in"))
    update(["w_in"])

    return (loss, grad_x[None], *[grads[k] for k in names], *[deltas[k] for k in names],
            *[new_m[k] for k in names], *[new_v[k] for k in names])
```

```python
import functools
import math

import jax
import jax.numpy as jnp
import numpy as np
from jax import lax
from jax.experimental import pallas as pl
from jax.experimental.pallas import tpu as pltpu

F32 = jnp.float32
BF16 = jnp.bfloat16
MESH = pl.DeviceIdType.MESH

HEAD_DIM = 128
HEADS_PER_GROUP = 4
DILATED_PATTERNS = ((128, 1), (512, 4), (2048, 16))
N_GROUPS = 3
N_HEADS = N_GROUPS * HEADS_PER_GROUP
SPAN = 128
GROUP_WIDTH = HEADS_PER_GROUP * HEAD_DIM
CONV_WIDTH = 31
FFN_CONV_WIDTH = 3
N_BUCKETS = 32
MAX_DISTANCE = 2048
RMS_EPS = 1e-6
LN_EPS = 1e-5
NEG_INF = -1e30
ADAM_LR = 0.001
ADAM_B1 = 0.9
ADAM_B2 = 0.999
ADAM_EPS = 1e-08
ADAM_WD = 0.01
ADAM_STEP = 10

LANES = 128
SUBLANES = 8
ROW_TILE = 256
TIME_BLOCK = 128
CONV_PAD = 32
FFN_PAD = 8
VMEM_LIMIT = 56 << 20


def _params(sem=None, vmem=None):
    kw = {}
    if sem is not None:
        kw["dimension_semantics"] = sem
    if vmem is not None:
        kw["vmem_limit_bytes"] = vmem
    return pltpu.CompilerParams(**kw)


def _pick(n, cands):
    for c in cands:
        if n % c == 0:
            return c
    return n


ELEMENTWISE_TILE_BYTES = 3 << 19


def _row_tile(rows, cols):
    for align in (16, SUBLANES):
        fits = [t for t in range(align, rows + 1, align) if rows % t == 0 and t * cols * 4 <= ELEMENTWISE_TILE_BYTES]
        if fits:
            return max(fits)
    return SUBLANES


N_CHIPS = 4
M_TILES = (1024, 1408, 512, 256, 128)
N_TILES = (1024, 512, 1408, 256, 128)
K_TILES = (2176, 2048, 1408, 1024, 512, 256, 128)


def _matmul(a, b, mode, name, out_shards=False, tm=None):
    assert a.dtype == BF16 and b.dtype == BF16, (name, a.dtype, b.dtype)
    b3 = b.ndim == 3
    tn = tk = None
    halves = None
    if mode == "nn":
        M, K = a.shape
        N = b.shape[-1] * (N_CHIPS if b3 else 1)
        tn = b.shape[-1] if b3 else None
    elif mode == "nt":
        if a.ndim == 3:
            halves = a.shape[2]
        M, K = a.shape[-2], a.shape[-1] * (a.shape[0] if a.ndim == 3 else 1)
        N = b.shape[-2]
        tk = b.shape[-1] if b3 else None
    else:
        if b3:
            halves = b.shape[2]
        K, M = a.shape
        N = b.shape[-1] * (b.shape[0] if b3 else 1)
        tn = N // N_CHIPS if out_shards else None
    tm = tm or _pick(M, M_TILES)
    tn = tn or _pick(N, N_TILES)
    tk = tk or _pick(K, K_TILES)
    nk = K // tk
    dn = {"nn": (((1,), (0,)), ((), ())), "nt": (((1,), (1,)), ((), ())), "tn": (((0,), (0,)), ((), ()))}[mode]

    def body(a_ref, b_ref, o_ref):
        if nk == 1:
            o_ref[...] = lax.dot_general(a_ref[...], b_ref[...], dn, preferred_element_type=F32)
        else:
            @pl.when(pl.program_id(2) == 0)
            def _():
                o_ref[...] = jnp.zeros_like(o_ref)

            o_ref[...] += lax.dot_general(a_ref[...], b_ref[...], dn, preferred_element_type=F32)

    if mode == "tn":
        a_spec = pl.BlockSpec((tk, tm), lambda i, j, k: (k, i))
    elif halves:
        per = halves // tk
        a_spec = pl.BlockSpec((None, tm, tk), lambda i, j, k: (k // per, i, k % per))
    else:
        a_spec = pl.BlockSpec((tm, tk), lambda i, j, k: (i, k))
    if mode == "nn":
        b_spec = pl.BlockSpec((None, tk, tn), lambda i, j, k: (j, k, 0)) if b3 else pl.BlockSpec((tk, tn), lambda i, j, k: (k, j))
    elif mode == "nt":
        b_spec = pl.BlockSpec((None, tn, tk), lambda i, j, k: (k, j, 0)) if b3 else pl.BlockSpec((tn, tk), lambda i, j, k: (j, k))
    elif halves:
        per = halves // tn
        b_spec = pl.BlockSpec((None, tk, tn), lambda i, j, k: (j // per, k, j % per))
    else:
        b_spec = pl.BlockSpec((tk, tn), lambda i, j, k: (k, j))
    if out_shards:
        out_spec = pl.BlockSpec((None, tm, tn), lambda i, j, k: (j, i, 0))
        out_shape = jax.ShapeDtypeStruct((N_CHIPS, M, tn), F32)
    else:
        out_spec = pl.BlockSpec((tm, tn), lambda i, j, k: (i, j))
        out_shape = jax.ShapeDtypeStruct((M, N), F32)
    return pl.pallas_call(
        body, name=name, grid=(M // tm, N // tn, nk),
        in_specs=[a_spec, b_spec], out_specs=out_spec, out_shape=out_shape,
        compiler_params=_params(("parallel", "parallel", "arbitrary"), VMEM_LIMIT),
    )(a, b)


def _rms(x, g):
    r = lax.rsqrt(jnp.mean(x * x, axis=-1, keepdims=True) + RMS_EPS)
    return x * r * g


def _rms_bwd(x, g, dy):
    r = lax.rsqrt(jnp.mean(x * x, axis=-1, keepdims=True) + RMS_EPS)
    n = x * r
    dn = dy * g
    dx = r * (dn - n * jnp.mean(dn * n, axis=-1, keepdims=True))
    return dx, jnp.sum(dy * n, axis=0, keepdims=True)


def _sigmoid(x):
    return 1.0 / (1.0 + jnp.exp(-x))


_GELU_C = math.sqrt(2.0 / math.pi)


def _gelu(x):
    return 0.5 * x * (1.0 + jnp.tanh(_GELU_C * (x + 0.044715 * x * x * x)))


def _gelu_and_grad(x):
    x2 = x * x
    t = jnp.tanh(_GELU_C * x * (1.0 + 0.044715 * x2))
    half = 0.5 * (1.0 + t)
    return x * half, half + (0.5 * _GELU_C) * x * (1.0 - t * t) * (1.0 + (3.0 * 0.044715) * x2)


def _row_spec(width, col_block=0):
    return pl.BlockSpec((ROW_TILE, width), lambda i: (i, col_block))


def _vec_spec(width, col_block=0):
    return pl.BlockSpec((1, width), lambda i: (0, col_block))


def _accumulate(ref, part):
    @pl.when(pl.program_id(0) == 0)
    def _():
        ref[...] = part

    @pl.when(pl.program_id(0) > 0)
    def _():
        ref[...] += part


def _rms_fwd_call(x, g):
    S, D = x.shape

    def body(x_ref, g_ref, h_ref):
        h_ref[...] = _rms(x_ref[...], g_ref[...]).astype(BF16)

    return pl.pallas_call(
        body, name="rms_mix_pre", grid=(S // ROW_TILE,),
        in_specs=[_row_spec(D), _vec_spec(D)], out_specs=_row_spec(D),
        out_shape=jax.ShapeDtypeStruct((S, D), BF16),
        compiler_params=_params(("parallel",)),
    )(x, g)


def _ln_silu_call(c1, g, b):
    S, C = c1.shape

    def body(c_ref, g_ref, b_ref, o_ref):
        xv = c_ref[...]
        mu = jnp.mean(xv, axis=-1, keepdims=True)
        xc = xv - mu
        var = jnp.mean(xc * xc, axis=-1, keepdims=True)
        z = xc * lax.rsqrt(var + LN_EPS) * g_ref[...] + b_ref[...]
        o_ref[...] = (z * _sigmoid(z)).astype(BF16)

    return pl.pallas_call(
        body, name="conv_ln_silu", grid=(S // ROW_TILE,),
        in_specs=[_row_spec(C), _vec_spec(C), _vec_spec(C)], out_specs=_row_spec(C),
        out_shape=jax.ShapeDtypeStruct((S, C), BF16),
        compiler_params=_params(("parallel",)),
    )(c1, g, b)


def _ln_silu_bwd_call(c1, g, b, dc):
    S, C = c1.shape

    def body(c_ref, g_ref, b_ref, dc_ref, dx_ref, dg_ref, db_ref):
        xv = c_ref[...]
        mu = jnp.mean(xv, axis=-1, keepdims=True)
        xc = xv - mu
        rs = lax.rsqrt(jnp.mean(xc * xc, axis=-1, keepdims=True) + LN_EPS)
        xh = xc * rs
        z = xh * g_ref[...] + b_ref[...]
        sg = _sigmoid(z)
        dz = dc_ref[...] * (sg * (1.0 + z * (1.0 - sg)))
        dxh = dz * g_ref[...]
        dx_ref[...] = rs * (dxh - jnp.mean(dxh, axis=-1, keepdims=True) - xh * jnp.mean(dxh * xh, axis=-1, keepdims=True))
        _accumulate(dg_ref, jnp.sum(dz * xh, axis=0, keepdims=True))
        _accumulate(db_ref, jnp.sum(dz, axis=0, keepdims=True))

    return pl.pallas_call(
        body, name="conv_ln_silu_bwd", grid=(S // ROW_TILE,),
        in_specs=[_row_spec(C), _vec_spec(C), _vec_spec(C), _row_spec(C)],
        out_specs=[_row_spec(C), _vec_spec(C), _vec_spec(C)],
        out_shape=[jax.ShapeDtypeStruct((S, C), F32), jax.ShapeDtypeStruct((1, C), F32), jax.ShapeDtypeStruct((1, C), F32)],
        compiler_params=_params(("arbitrary",)),
    )(c1, g, b, dc)


def _mix_call(proj, gate_col0, b_gate, y_a, y_c):
    S, D = y_a.shape
    w = 512
    nc = D // w
    ga0, gc0 = gate_col0 // w, (gate_col0 + D) // w

    def body(ga_ref, gc_ref, ba_ref, bc_ref, ya_ref, yc_ref, o_ref):
        o_ref[...] = (_sigmoid(ga_ref[...] + ba_ref[...]) * ya_ref[...]
                      + _sigmoid(gc_ref[...] + bc_ref[...]) * yc_ref[...]).astype(BF16)

    tile = lambda off: pl.BlockSpec((ROW_TILE, w), lambda i, j: (i, off + j))
    vec = lambda off: pl.BlockSpec((1, w), lambda i, j: (0, off + j))
    return pl.pallas_call(
        body, name="gate_mix", grid=(S // ROW_TILE, nc),
        in_specs=[tile(ga0), tile(gc0), vec(0), vec(nc), tile(0), tile(0)],
        out_specs=tile(0), out_shape=jax.ShapeDtypeStruct((S, D), BF16),
        compiler_params=_params(("parallel", "parallel")),
    )(proj, proj, b_gate, b_gate, y_a, y_c)


def _mix_bwd_call(dmixed, proj, gate_col0, b_gate, y_a, y_c):
    S, D = y_a.shape
    w = 512
    nc = D // w
    ga0, gc0 = gate_col0 // w, (gate_col0 + D) // w

    def body(dm_ref, ga_ref, gc_ref, ba_ref, bc_ref, ya_ref, yc_ref, dya_ref, dyc_ref, dga_ref, dgc_ref, dba_ref, dbc_ref):
        dm = dm_ref[...]
        sa = _sigmoid(ga_ref[...] + ba_ref[...])
        sc = _sigmoid(gc_ref[...] + bc_ref[...])
        dya_ref[...] = (dm * sa).astype(BF16)
        dyc_ref[...] = (dm * sc).astype(BF16)
        dga = dm * ya_ref[...] * sa * (1.0 - sa)
        dgc = dm * yc_ref[...] * sc * (1.0 - sc)
        dga_ref[...] = dga.astype(BF16)
        dgc_ref[...] = dgc.astype(BF16)
        pa = jnp.sum(dga, axis=0, keepdims=True)
        pc = jnp.sum(dgc, axis=0, keepdims=True)

        @pl.when(pl.program_id(1) == 0)
        def _():
            dba_ref[...] = pa
            dbc_ref[...] = pc

        @pl.when(pl.program_id(1) > 0)
        def _():
            dba_ref[...] += pa
            dbc_ref[...] += pc

    tile = lambda off: pl.BlockSpec((ROW_TILE, w), lambda j, i: (i, off + j))
    vec = lambda off: pl.BlockSpec((1, w), lambda j, i: (0, off + j))
    return pl.pallas_call(
        body, name="gate_mix_bwd", grid=(nc, S // ROW_TILE),
        in_specs=[tile(0), tile(ga0), tile(gc0), vec(0), vec(nc), tile(0), tile(0)],
        out_specs=[tile(0), tile(0), tile(0), tile(0), vec(0), vec(0)],
        out_shape=[jax.ShapeDtypeStruct((S, D), BF16)] * 4 + [
                   jax.ShapeDtypeStruct((1, D), F32), jax.ShapeDtypeStruct((1, D), F32)],
        compiler_params=_params(("parallel", "arbitrary")),
    )(dmixed, proj, proj, b_gate, b_gate, y_a, y_c)


def _res1_call(x, out, g_post, g_pre):
    S, D = x.shape

    def body(x_ref, o_ref, gp_ref, gq_ref, x1_ref, h2_ref):
        x1 = x_ref[...] + _rms(o_ref[...], gp_ref[...])
        x1_ref[...] = x1
        h2_ref[...] = _rms(x1, gq_ref[...]).astype(BF16)

    return pl.pallas_call(
        body, name="residual_mix", grid=(S // ROW_TILE,),
        in_specs=[_row_spec(D), _row_spec(D), _vec_spec(D), _vec_spec(D)],
        out_specs=[_row_spec(D), _row_spec(D)],
        out_shape=[jax.ShapeDtypeStruct((S, D), F32), jax.ShapeDtypeStruct((S, D), BF16)],
        compiler_params=_params(("parallel",)),
    )(x, out, g_post, g_pre)


def _loss_call(y, x1, g_post, target):
    S, D = y.shape

    def body(y_ref, x1_ref, g_ref, t_ref, loss_ref, dx_ref):
        err = x1_ref[...] + _rms(y_ref[...], g_ref[...]) - t_ref[...]
        dx_ref[...] = err * (1.0 / D)
        part = 0.5 * jnp.sum(jnp.mean(err * err, axis=-1, keepdims=True), axis=0, keepdims=True)
        _accumulate(loss_ref, jnp.broadcast_to(part, (SUBLANES, LANES)))

    return pl.pallas_call(
        body, name="residual_ffn_loss", grid=(S // ROW_TILE,),
        in_specs=[_row_spec(D), _row_spec(D), _vec_spec(D), _row_spec(D)],
        out_specs=[pl.BlockSpec((SUBLANES, LANES), lambda i: (0, 0)), _row_spec(D)],
        out_shape=[jax.ShapeDtypeStruct((SUBLANES, LANES), F32), jax.ShapeDtypeStruct((S, D), F32)],
        compiler_params=_params(("arbitrary",)),
    )(y, x1, g_post, target)


def _rms_bwd_call(x, g, dy, name):
    S, D = x.shape

    def body(x_ref, g_ref, dy_ref, dx_ref, dg_ref):
        dx, dg = _rms_bwd(x_ref[...], g_ref[...], dy_ref[...])
        dx_ref[...] = dx.astype(BF16)
        _accumulate(dg_ref, dg)

    return pl.pallas_call(
        body, name=name, grid=(S // ROW_TILE,),
        in_specs=[_row_spec(D), _vec_spec(D), _row_spec(D)],
        out_specs=[_row_spec(D), _vec_spec(D)],
        out_shape=[jax.ShapeDtypeStruct((S, D), BF16), jax.ShapeDtypeStruct((1, D), F32)],
        compiler_params=_params(("arbitrary",)),
    )(x, g, dy)


def _mid_bwd_call(x1, g_pre, dh2, dx2, out, g_post):
    S, D = x1.shape

    def body(x1_ref, gq_ref, dh_ref, dx2_ref, o_ref, gp_ref, dx1_ref, do_ref, dgq_ref, dgp_ref):
        d, dgq = _rms_bwd(x1_ref[...], gq_ref[...], dh_ref[...])
        dx1 = dx2_ref[...] + d
        dx1_ref[...] = dx1
        do, dgp = _rms_bwd(o_ref[...], gp_ref[...], dx1)
        do_ref[...] = do.astype(BF16)
        _accumulate(dgq_ref, dgq)
        _accumulate(dgp_ref, dgp)

    return pl.pallas_call(
        body, name="residual_mix_bwd", grid=(S // ROW_TILE,),
        in_specs=[_row_spec(D), _vec_spec(D), _row_spec(D), _row_spec(D), _row_spec(D), _vec_spec(D)],
        out_specs=[_row_spec(D), _row_spec(D), _vec_spec(D), _vec_spec(D)],
        out_shape=[jax.ShapeDtypeStruct((S, D), F32), jax.ShapeDtypeStruct((S, D), BF16)] + [jax.ShapeDtypeStruct((1, D), F32)] * 2,
        compiler_params=_params(("arbitrary",)),
    )(x1, g_pre, dh2, dx2, out, g_post)


def _in_bwd_call(x, g, dh1, dx1):
    S, D = x.shape

    def body(x_ref, g_ref, dh_ref, dx1_ref, gx_ref, dg_ref):
        d, dg = _rms_bwd(x_ref[...], g_ref[...], dh_ref[...])
        gx_ref[...] = dx1_ref[...] + d
        _accumulate(dg_ref, dg)

    return pl.pallas_call(
        body, name="rms_mix_pre_bwd", grid=(S // ROW_TILE,),
        in_specs=[_row_spec(D), _vec_spec(D), _row_spec(D), _row_spec(D)],
        out_specs=[_row_spec(D), _vec_spec(D)],
        out_shape=[jax.ShapeDtypeStruct((S, D), F32), jax.ShapeDtypeStruct((1, D), F32)],
        compiler_params=_params(("arbitrary",)),
    )(x, g, dh1, dx1)


def _bucket_table(dilation):
    qi = np.arange(SPAN)[:, None]
    ki = np.arange(2 * SPAN)[None, :]
    dist = np.maximum(qi + SPAN - ki, 0) * dilation
    max_exact = N_BUCKETS // 2
    d = np.maximum(dist, 1).astype(np.float64)
    large = max_exact + (np.log(d / max_exact) / math.log(MAX_DISTANCE / max_exact) * (N_BUCKETS - max_exact)).astype(np.int32)
    large = np.minimum(large, N_BUCKETS - 1)
    return np.where(dist < max_exact, dist, large).astype(np.int32)


def _bucket_tables():
    return jnp.asarray(np.stack([_bucket_table(r) for _, r in DILATED_PATTERNS]))


def _bias_table_call(rel_bias, buckets):
    def body(rb_ref, bk_ref, o_ref):
        for h in range(N_HEADS):
            bk = bk_ref[h // HEADS_PER_GROUP]

            def step(b, acc):
                return jnp.where(bk == b, rb_ref[b, h], acc)

            o_ref[h] = lax.fori_loop(0, N_BUCKETS, step, jnp.zeros((SPAN, 2 * SPAN), F32))

    return pl.pallas_call(
        body, name="rel_bias_table",
        in_specs=[pl.BlockSpec(memory_space=pltpu.SMEM), pl.BlockSpec(memory_space=pltpu.VMEM)],
        out_specs=pl.BlockSpec(memory_space=pltpu.VMEM),
        out_shape=jax.ShapeDtypeStruct((N_HEADS, SPAN, 2 * SPAN), F32),
    )(rel_bias, buckets)


def _bias_grad_call(dbias, buckets):
    def body(db_ref, bk_ref, o_ref, rows_ref):
        for h in range(N_HEADS):
            bk = bk_ref[h // HEADS_PER_GROUP]
            dv = db_ref[h]

            def step(b, carry):
                rows_ref[h, b] = jnp.sum(jnp.where(bk == b, dv, 0.0), axis=0, keepdims=True)
                return carry

            lax.fori_loop(0, N_BUCKETS, step, 0)
        o_ref[...] = jnp.sum(rows_ref[...], axis=-1, keepdims=True)

    out = pl.pallas_call(
        body, name="rel_bias_grad",
        in_specs=[pl.BlockSpec(memory_space=pltpu.VMEM), pl.BlockSpec(memory_space=pltpu.VMEM)],
        out_specs=pl.BlockSpec(memory_space=pltpu.VMEM),
        out_shape=jax.ShapeDtypeStruct((N_HEADS, N_BUCKETS, 1, 1), F32),
        scratch_shapes=[pltpu.VMEM((N_HEADS, N_BUCKETS, 1, 2 * SPAN), F32)],
    )(dbias, buckets)
    return out.reshape(N_HEADS, N_BUCKETS).T


def _dot_nt(a, b):
    return lax.dot_general(a, b, (((1,), (1,)), ((), ())), preferred_element_type=F32)


def _dot_nn(a, b):
    return lax.dot_general(a, b, (((1,), (0,)), ((), ())), preferred_element_type=F32)


def _dot_tn(a, b):
    return lax.dot_general(a, b, (((0,), (0,)), ((), ())), preferred_element_type=F32)


def _band_masks(n, nb):
    qi = lax.broadcasted_iota(jnp.int32, (SPAN, SPAN), 0)
    ki = lax.broadcasted_iota(jnp.int32, (SPAN, SPAN), 1)
    prev_ok = jnp.logical_and(ki >= qi, n > 0)
    cur_ok = ki <= qi
    next_ok = jnp.logical_and(ki >= qi, n < nb - 1)
    return prev_ok, cur_ok, next_ok


def _wide_band_mask(n):
    qi = lax.broadcasted_iota(jnp.int32, (SPAN, 2 * SPAN), 0)
    ki = lax.broadcasted_iota(jnp.int32, (SPAN, 2 * SPAN), 1)
    prev_ok = jnp.logical_and(jnp.logical_and(ki < SPAN, ki >= qi), n > 0)
    cur_ok = jnp.logical_and(ki >= SPAN, ki - SPAN <= qi)
    return jnp.logical_or(prev_ok, cur_ok)


def _attn_plan(S, group):
    r = DILATED_PATTERNS[group][1]
    hp, per = (HEADS_PER_GROUP, 1) if r == 1 else (2, 4)
    return r, S // (r * SPAN), hp, per


def _residue_rows(rho, r):
    return slice(None) if r == 1 else pl.ds(rho, SPAN, stride=r)


def _for_residues(r, per, fn):
    if r == per:
        for u in range(per):
            fn(u)
        return

    def step(i, carry):
        for u in range(per):
            fn(i * per + u)
        return carry

    lax.fori_loop(0, r // per, step, 0)


def _attn_fwd_call(proj, bias, group):
    S = proj.shape[0]
    r, nb, hp, per = _attn_plan(S, group)
    scale = HEAD_DIM ** -0.5
    kinds = ("q", "kp", "kc", "vp", "vc") if nb > 1 else ("q", "kc", "vc")

    def body(*refs):
        ins = {kind: refs[i * hp:(i + 1) * hp] for i, kind in enumerate(kinds)}
        b_ref, o_ref, lse_ref = refs[len(kinds) * hp:]
        n = pl.program_id(1)
        prev_ok, cur_ok, _ = _band_masks(n, nb)

        band_ok = _wide_band_mask(n) if nb > 1 else cur_ok

        def residue(rho):
            rows = _residue_rows(rho, r)
            for j in range(hp):
                get = lambda kind: ins[kind][j][rows, :].astype(BF16)
                q = get("q")
                if nb > 1:
                    keys, vals, bias_j = jnp.concatenate([get("kp"), get("kc")], axis=0), jnp.concatenate([get("vp"), get("vc")], axis=0), b_ref[j]
                else:
                    keys, vals, bias_j = get("kc"), get("vc"), b_ref[j, :, SPAN:]
                s = jnp.where(band_ok, _dot_nt(q, keys) * scale + bias_j, NEG_INF)
                m = jnp.max(s, axis=-1, keepdims=True)
                p = jnp.exp(s - m)
                den = jnp.sum(p, axis=-1, keepdims=True)
                o_ref[j, rows, :] = _dot_nn(p.astype(BF16), vals) / den
                lse_ref[j, rows, :] = jnp.broadcast_to(m + jnp.log(den), (SPAN, HEAD_DIM))

        _for_residues(r, per, residue)

    in_specs = [_head_spec(r, nb, hp, kind, group, jj) for kind in kinds for jj in range(hp)]
    in_specs.append(pl.BlockSpec((hp, SPAN, 2 * SPAN), lambda j, n: (group * (HEADS_PER_GROUP // hp) + j, 0, 0)))
    out = pl.BlockSpec((hp, r * SPAN, HEAD_DIM), lambda j, n: (j, n, 0))
    return pl.pallas_call(
        body, name=f"attn_fwd_g{group}", grid=(HEADS_PER_GROUP // hp, nb),
        in_specs=in_specs, out_specs=[out] * 2,
        out_shape=[jax.ShapeDtypeStruct((HEADS_PER_GROUP, S, HEAD_DIM), F32)] * 2,
        compiler_params=_params(("parallel", "parallel"), VMEM_LIMIT),
    )(*([proj] * (len(in_specs) - 1)), bias)


_PROJ_PART = dict(q=0, qn=0, kp=1, kc=1, vp=2, vc=2)


def _head_spec(r, nb, hp, kind, group, jj):
    if kind in _PROJ_PART:
        base = (_PROJ_PART[kind] * N_GROUPS + group) * HEADS_PER_GROUP
    else:
        base = 0
    if kind.endswith("p"):
        row = lambda n: jnp.maximum(n - 1, 0)
    elif kind.endswith("n"):
        row = lambda n: jnp.minimum(n + 1, nb - 1)
    else:
        row = lambda n: n
    return pl.BlockSpec((r * SPAN, HEAD_DIM), lambda j, n: (row(n), base + j * hp + jj))


def _attn_merge_call(parts):
    S = parts[0].shape[1]

    def body(o1, s1, o2, s2, o3, s3, a_ref, ab_ref, lse_ref):
        for j in range(HEADS_PER_GROUP):
            sl = slice(j * HEAD_DIM, (j + 1) * HEAD_DIM)
            mx = jnp.maximum(jnp.maximum(s1[j], s2[j]), s3[j])
            w1 = jnp.exp(s1[j] - mx)
            w2 = jnp.exp(s2[j] - mx)
            w3 = jnp.exp(s3[j] - mx)
            den = w1 + w2 + w3
            a = (w1 * o1[j] + w2 * o2[j] + w3 * o3[j]) / den
            a_ref[:, sl] = a
            ab_ref[:, sl] = a.astype(BF16)
            lse_ref[:, sl] = mx + jnp.log(den)

    heads = pl.BlockSpec((HEADS_PER_GROUP, ROW_TILE, HEAD_DIM), lambda i: (0, i, 0))
    return pl.pallas_call(
        body, name="attn_merge", grid=(S // ROW_TILE,),
        in_specs=[heads] * 6, out_specs=[_row_spec(GROUP_WIDTH)] * 3,
        out_shape=[jax.ShapeDtypeStruct((S, GROUP_WIDTH), F32), jax.ShapeDtypeStruct((S, GROUP_WIDTH), BF16),
                   jax.ShapeDtypeStruct((S, GROUP_WIDTH), F32)],
        compiler_params=_params(("parallel",)),
    )(*parts)


def _attn_delta_call(a, da):
    S = a.shape[0]

    def body(a_ref, da_ref, d_ref):
        for j in range(HEADS_PER_GROUP):
            sl = slice(j * HEAD_DIM, (j + 1) * HEAD_DIM)
            d = jnp.sum(a_ref[:, sl] * da_ref[:, sl], axis=-1, keepdims=True)
            d_ref[:, sl] = jnp.broadcast_to(d, (ROW_TILE, HEAD_DIM))

    return pl.pallas_call(
        body, name="attn_delta", grid=(S // ROW_TILE,),
        in_specs=[_row_spec(GROUP_WIDTH)] * 2, out_specs=_row_spec(GROUP_WIDTH),
        out_shape=jax.ShapeDtypeStruct((S, GROUP_WIDTH), F32),
        compiler_params=_params(("parallel",)),
    )(a, da)


def _attn_bwd_call(proj, bias, da, lse, delta, group):
    S = proj.shape[0]
    r, nb, hp, per = _attn_plan(S, group)
    scale = HEAD_DIM ** -0.5
    kinds = ("q", "qn", "kp", "kc", "vp", "vc", "da", "dan", "lse", "lsen", "dl", "dln") if nb > 1 else ("q", "kc", "vc", "da", "lse", "dl")
    source = dict(da=da, dan=da, lse=lse, lsen=lse, dl=delta, dln=delta)

    def body(*refs):
        ins = {kind: refs[i * hp:(i + 1) * hp] for i, kind in enumerate(kinds)}
        b_ref, dq_ref, dk_ref, dv_ref, db_ref = refs[len(kinds) * hp:]
        n = pl.program_id(1)
        prev_ok, cur_ok, next_ok = _band_masks(n, nb)

        @pl.when(n == 0)
        def _():
            db_ref[...] = jnp.zeros_like(db_ref)

        band_ok = _wide_band_mask(n) if nb > 1 else cur_ok

        def residue(rho):
            rows = _residue_rows(rho, r)
            for j in range(hp):
                get = lambda kind: ins[kind][j][rows, :]
                q = get("q").astype(BF16)
                kc = get("kc").astype(BF16)
                vc = get("vc").astype(BF16)
                dav = get("da").astype(BF16)
                lse_q, dl_q = get("lse"), get("dl")
                if nb == 1:
                    pc = jnp.exp(jnp.where(cur_ok, _dot_nt(q, kc) * scale + b_ref[j, :, SPAN:], NEG_INF) - lse_q)
                    dsc = pc * (_dot_nt(dav, vc) - dl_q)
                    dsc_b = dsc.astype(BF16)
                    dq = _dot_nn(dsc_b, kc)
                    dk = _dot_tn(dsc_b, q)
                    dv = _dot_tn(pc.astype(BF16), dav)
                    db_ref[j, :, SPAN:] += dsc
                else:
                    qn = get("qn").astype(BF16)
                    dan = get("dan").astype(BF16)
                    keys = jnp.concatenate([get("kp").astype(BF16), kc], axis=0)
                    vals = jnp.concatenate([get("vp").astype(BF16), vc], axis=0)
                    wide = lambda t: jnp.concatenate([t, t], axis=1)
                    p = jnp.exp(jnp.where(band_ok, _dot_nt(q, keys) * scale + b_ref[j], NEG_INF) - wide(lse_q))
                    ds = p * (_dot_nt(dav, vals) - wide(dl_q))
                    dq = _dot_nn(ds.astype(BF16), keys)
                    db_ref[j] += ds
                    pn = jnp.exp(jnp.where(next_ok, _dot_nt(qn, kc) * scale + b_ref[j, :, :SPAN], NEG_INF) - get("lsen"))
                    dsn = pn * (_dot_nt(dan, vc) - get("dln"))
                    both = lambda cur_part, next_part: jnp.concatenate([cur_part.astype(BF16), next_part.astype(BF16)], axis=0)
                    dk = _dot_tn(both(ds[:, SPAN:], dsn), jnp.concatenate([q, qn], axis=0))
                    dv = _dot_tn(both(p[:, SPAN:], pn), jnp.concatenate([dav, dan], axis=0))
                dq_ref[j, rows, :] = dq * scale
                dk_ref[j, rows, :] = dk * scale
                dv_ref[j, rows, :] = dv

        _for_residues(r, per, residue)

    per_group = HEADS_PER_GROUP // hp
    band = (hp, SPAN, 2 * SPAN)
    in_specs = [_head_spec(r, nb, hp, kind, group, jj) for kind in kinds for jj in range(hp)]
    in_specs.append(pl.BlockSpec(band, lambda j, n: (group * per_group + j, 0, 0)))
    operands = [source.get(kind, proj) for kind in kinds for _ in range(hp)] + [bias]
    out = pl.BlockSpec((hp, r * SPAN, HEAD_DIM), lambda j, n: (j, n, 0))
    return pl.pallas_call(
        body, name=f"attn_bwd_g{group}", grid=(per_group, nb),
        in_specs=in_specs,
        out_specs=[out] * 3 + [pl.BlockSpec(band, lambda j, n: (j, 0, 0))],
        out_shape=[jax.ShapeDtypeStruct((HEADS_PER_GROUP, S, HEAD_DIM), F32)] * 3
        + [jax.ShapeDtypeStruct((HEADS_PER_GROUP, SPAN, 2 * SPAN), F32)],
        compiler_params=_params(("parallel", "arbitrary"), VMEM_LIMIT),
    )(*operands)


def _dproj_call(dqkv, tails):
    S = tails[0].shape[0]
    width = len(dqkv) * GROUP_WIDTH + sum(t.shape[1] for t in tails)

    def body(*refs):
        o_ref = refs[-1]
        col = 0
        for ref in refs[:len(dqkv)]:
            for j in range(HEADS_PER_GROUP):
                o_ref[:, col:col + HEAD_DIM] = ref[j].astype(BF16)
                col += HEAD_DIM
        for ref in refs[len(dqkv):-1]:
            o_ref[:, col:col + ref.shape[1]] = ref[...]
            col += ref.shape[1]

    heads = pl.BlockSpec((HEADS_PER_GROUP, ROW_TILE, HEAD_DIM), lambda i: (0, i, 0))
    return pl.pallas_call(
        body, name="dproj_assemble", grid=(S // ROW_TILE,),
        in_specs=[heads] * len(dqkv) + [_row_spec(t.shape[1]) for t in tails],
        out_specs=_row_spec(width), out_shape=jax.ShapeDtypeStruct((S, width), BF16),
        compiler_params=_params(("parallel",)),
    )(*dqkv, *tails)


def _tap_rows(xpad_ref, t0, k, width, pad):
    return xpad_ref[pl.ds(t0 + (pad - (width - 1 - k)), TIME_BLOCK), :]


def _conv_block(xpad_ref, t0, w_ref, width, pad):
    acc = None
    for k in range(width):
        term = w_ref[k:k + 1, :] * _tap_rows(xpad_ref, t0, k, width, pad)
        acc = term if acc is None else acc + term
    return acc


def _conv_transpose_block(dpad_ref, t0, w_ref, width):
    acc = None
    for k in range(width):
        term = w_ref[k:k + 1, :] * dpad_ref[pl.ds(t0 + (width - 1 - k), TIME_BLOCK), :]
        acc = term if acc is None else acc + term
    return acc


def _conv_weight_grad(xpad_ref, t0, dy, dw_ref, width, pad):
    for k in range(width):
        dw_ref[k:k + 1, :] += jnp.sum(dy * _tap_rows(xpad_ref, t0, k, width, pad), axis=0, keepdims=True)


def _time_loop(S, step):
    def it(tb, carry):
        step(pl.multiple_of(tb * TIME_BLOCK, TIME_BLOCK))
        return carry

    lax.fori_loop(0, S // TIME_BLOCK, it, 0)


def _conv_fwd_call(proj, col0, w, b):
    S = proj.shape[0]
    C = w.shape[1]
    nt = C // LANES
    v0, g0 = col0 // LANES, (col0 + C) // LANES

    def body(val_ref, gate_ref, w_ref, b_ref, o_ref, pad_ref):
        pad_ref[0:CONV_PAD, :] = jnp.zeros((CONV_PAD, LANES), F32)
        pad_ref[CONV_PAD:, :] = val_ref[...] * _sigmoid(gate_ref[...])

        def step(t0):
            o_ref[pl.ds(t0, TIME_BLOCK), :] = _conv_block(pad_ref, t0, w_ref, CONV_WIDTH, CONV_PAD) + b_ref[...]

        _time_loop(S, step)

    seq = lambda off: pl.BlockSpec((S, LANES), lambda i: (0, off + i))
    return pl.pallas_call(
        body, name="conv_module", grid=(nt,),
        in_specs=[seq(v0), seq(g0), pl.BlockSpec((CONV_WIDTH, LANES), lambda i: (0, i)), pl.BlockSpec((1, LANES), lambda i: (0, i))],
        out_specs=seq(0), out_shape=jax.ShapeDtypeStruct((S, C), F32),
        scratch_shapes=[pltpu.VMEM((S + CONV_PAD, LANES), F32)],
        compiler_params=_params(("parallel",)),
    )(proj, proj, w, b)


def _conv_bwd_call(proj, col0, w, dc1):
    S = proj.shape[0]
    C = w.shape[1]
    nt = C // LANES
    v0, g0 = col0 // LANES, (col0 + C) // LANES

    def body(val_ref, gate_ref, w_ref, dy_ref, dval_ref, dgate_ref, dw_ref, db_ref, xpad_ref, dpad_ref, dwacc_ref):
        xpad_ref[0:CONV_PAD, :] = jnp.zeros((CONV_PAD, LANES), F32)
        xpad_ref[CONV_PAD:, :] = val_ref[...] * _sigmoid(gate_ref[...])
        dpad_ref[0:S, :] = dy_ref[...]
        dpad_ref[S:, :] = jnp.zeros((CONV_PAD, LANES), F32)
        dwacc_ref[...] = jnp.zeros_like(dwacc_ref)

        def step(t0):
            rows = pl.ds(t0, TIME_BLOCK)
            _conv_weight_grad(xpad_ref, t0, dy_ref[rows, :], dwacc_ref, CONV_WIDTH, CONV_PAD)
            dc0 = _conv_transpose_block(dpad_ref, t0, w_ref, CONV_WIDTH)
            sg = _sigmoid(gate_ref[rows, :])
            dval_ref[rows, :] = (dc0 * sg).astype(BF16)
            dgate_ref[rows, :] = (dc0 * val_ref[rows, :] * sg * (1.0 - sg)).astype(BF16)

        _time_loop(S, step)
        dw_ref[...] = dwacc_ref[...]
        db_ref[...] = jnp.sum(dy_ref[...], axis=0, keepdims=True)

    seq = lambda off: pl.BlockSpec((S, LANES), lambda i: (0, off + i))
    return pl.pallas_call(
        body, name="conv_module_bwd", grid=(nt,),
        in_specs=[seq(v0), seq(g0), pl.BlockSpec((CONV_WIDTH, LANES), lambda i: (0, i)), seq(0)],
        out_specs=[seq(0), seq(0), pl.BlockSpec((CONV_PAD, LANES), lambda i: (0, i)), pl.BlockSpec((1, LANES), lambda i: (0, i))],
        out_shape=[jax.ShapeDtypeStruct((S, C), BF16), jax.ShapeDtypeStruct((S, C), BF16),
                   jax.ShapeDtypeStruct((CONV_PAD, C), F32), jax.ShapeDtypeStruct((1, C), F32)],
        scratch_shapes=[pltpu.VMEM((S + CONV_PAD, LANES), F32), pltpu.VMEM((S + CONV_PAD, LANES), F32),
                        pltpu.VMEM((CONV_PAD, LANES), F32)],
        compiler_params=_params(("parallel",)),
    )(proj, proj, w, dc1)


def _ffn_fwd_call(u, w, b):
    S, C2 = u.shape
    C = C2 // 2
    nt = C // LANES

    def body(ug_ref, uv_ref, wg_ref, wv_ref, bg_ref, bv_ref, f_ref, pg_ref, pv_ref):
        zeros = jnp.zeros((FFN_PAD, LANES), F32)
        pg_ref[0:FFN_PAD, :] = zeros
        pv_ref[0:FFN_PAD, :] = zeros
        pg_ref[FFN_PAD:, :] = ug_ref[...]
        pv_ref[FFN_PAD:, :] = uv_ref[...]

        def step(t0):
            cg = _conv_block(pg_ref, t0, wg_ref, FFN_CONV_WIDTH, FFN_PAD) + bg_ref[...]
            cv = _conv_block(pv_ref, t0, wv_ref, FFN_CONV_WIDTH, FFN_PAD) + bv_ref[...]
            f_ref[pl.ds(t0, TIME_BLOCK), :] = (_gelu(cg) * cv).astype(BF16)

        _time_loop(S, step)

    seq = lambda off: pl.BlockSpec((S, LANES), lambda i: (0, off + i))
    wsp = lambda off: pl.BlockSpec((FFN_CONV_WIDTH, LANES), lambda i: (0, off + i))
    bsp = lambda off: pl.BlockSpec((1, LANES), lambda i: (0, off + i))
    return pl.pallas_call(
        body, name="ffn_conv_geglu", grid=(nt,),
        in_specs=[seq(0), seq(nt), wsp(0), wsp(nt), bsp(0), bsp(nt)],
        out_specs=seq(0), out_shape=jax.ShapeDtypeStruct((S, C), BF16),
        scratch_shapes=[pltpu.VMEM((S + FFN_PAD, LANES), F32)] * 2,
        compiler_params=_params(("parallel",)),
    )(u, u, w, w, b, b)


def _ffn_bwd_call(u, w, b, df):
    S, C2 = u.shape
    C = C2 // 2
    nt = C // LANES

    def body(ug_ref, uv_ref, wg_ref, wv_ref, bg_ref, bv_ref, df_ref,
             du_ref, dwg_ref, dwv_ref, dbg_ref, dbv_ref,
             pg_ref, pv_ref, dg_ref, dv_ref, dwg_acc, dwv_acc, dbg_acc, dbv_acc):
        zeros = jnp.zeros((FFN_PAD, LANES), F32)
        pg_ref[0:FFN_PAD, :] = zeros
        pv_ref[0:FFN_PAD, :] = zeros
        pg_ref[FFN_PAD:, :] = ug_ref[...]
        pv_ref[FFN_PAD:, :] = uv_ref[...]
        dg_ref[S:, :] = zeros
        dv_ref[S:, :] = zeros
        dwg_acc[...] = jnp.zeros_like(dwg_acc)
        dwv_acc[...] = jnp.zeros_like(dwv_acc)
        dbg_acc[...] = jnp.zeros_like(dbg_acc)
        dbv_acc[...] = jnp.zeros_like(dbv_acc)

        def first(t0):
            rows = pl.ds(t0, TIME_BLOCK)
            cg = _conv_block(pg_ref, t0, wg_ref, FFN_CONV_WIDTH, FFN_PAD) + bg_ref[...]
            cv = _conv_block(pv_ref, t0, wv_ref, FFN_CONV_WIDTH, FFN_PAD) + bv_ref[...]
            dfb = df_ref[rows, :]
            gelu, gelu_grad = _gelu_and_grad(cg)
            dcg = dfb * cv * gelu_grad
            dcv = dfb * gelu
            dg_ref[rows, :] = dcg
            dv_ref[rows, :] = dcv
            _conv_weight_grad(pg_ref, t0, dcg, dwg_acc, FFN_CONV_WIDTH, FFN_PAD)
            _conv_weight_grad(pv_ref, t0, dcv, dwv_acc, FFN_CONV_WIDTH, FFN_PAD)
            dbg_acc[...] += jnp.sum(dcg, axis=0, keepdims=True)
            dbv_acc[...] += jnp.sum(dcv, axis=0, keepdims=True)

        def second(t0):
            rows = pl.ds(t0, TIME_BLOCK)
            du_ref[0, rows, :] = _conv_transpose_block(dg_ref, t0, wg_ref, FFN_CONV_WIDTH).astype(BF16)
            du_ref[1, rows, :] = _conv_transpose_block(dv_ref, t0, wv_ref, FFN_CONV_WIDTH).astype(BF16)

        _time_loop(S, first)
        _time_loop(S, second)
        dwg_ref[...] = dwg_acc[...]
        dwv_ref[...] = dwv_acc[...]
        dbg_ref[...] = dbg_acc[...]
        dbv_ref[...] = dbv_acc[...]

    seq = lambda off: pl.BlockSpec((S, LANES), lambda i: (0, off + i))
    wsp = lambda off: pl.BlockSpec((FFN_CONV_WIDTH, LANES), lambda i: (0, off + i))
    bsp = lambda off: pl.BlockSpec((1, LANES), lambda i: (0, off + i))
    return pl.pallas_call(
        body, name="ffn_conv_geglu_bwd", grid=(nt,),
        in_specs=[seq(0), seq(nt), wsp(0), wsp(nt), bsp(0), bsp(nt), seq(0)],
        out_specs=[pl.BlockSpec((2, S, LANES), lambda i: (0, 0, i)),
                   pl.BlockSpec((SUBLANES, LANES), lambda i: (0, i)), pl.BlockSpec((SUBLANES, LANES), lambda i: (0, i)),
                   bsp(0), bsp(0)],
        out_shape=[jax.ShapeDtypeStruct((2, S, C), BF16)] + [jax.ShapeDtypeStruct((SUBLANES, C), F32)] * 2
        + [jax.ShapeDtypeStruct((1, C), F32)] * 2,
        scratch_shapes=[pltpu.VMEM((S + FFN_PAD, LANES), F32)] * 4 + [pltpu.VMEM((SUBLANES, LANES), F32)] * 2
        + [pltpu.VMEM((1, LANES), F32)] * 2,
        compiler_params=_params(("parallel",)),
    )(u, u, w, w, b, b, df)


def _adamw(w_ref, g_ref, m_ref, v_ref, d_ref, mo_ref, vo_ref):
    gv = g_ref[...]
    mn = ADAM_B1 * m_ref[...] + (1.0 - ADAM_B1) * gv
    vn = ADAM_B2 * v_ref[...] + (1.0 - ADAM_B2) * (gv * gv)
    mo_ref[...] = mn
    vo_ref[...] = vn
    m_hat = mn * (1.0 / (1.0 - ADAM_B1 ** ADAM_STEP))
    v_hat = vn * (1.0 / (1.0 - ADAM_B2 ** ADAM_STEP))
    d_ref[...] = -ADAM_LR * (m_hat / (jnp.sqrt(v_hat) + ADAM_EPS) + ADAM_WD * w_ref[...])


def _adamw_call(w, g, m, v, name):
    R, C = w.shape
    tr = _row_tile(R, C)
    spec = pl.BlockSpec((tr, C), lambda i: (i, 0))
    return pl.pallas_call(
        _adamw_body(), name=name, grid=(R // tr,),
        in_specs=[spec] * 4, out_specs=[spec] * 3,
        out_shape=[jax.ShapeDtypeStruct((R, C), F32)] * 3,
        compiler_params=_params(("parallel",)),
    )(w, g, m, v)


def _adamw_body():
    def body(*refs):
        _adamw(*refs)

    return body


def _adamw_small_call(ws, gs, ms, vs):
    n = len(ws)

    def body(*refs):
        w_refs, g_refs, m_refs, v_refs, d_refs, mo_refs, vo_refs = (refs[i * n:(i + 1) * n] for i in range(7))
        for i in range(n):
            _adamw(w_refs[i], g_refs[i], m_refs[i], v_refs[i], d_refs[i], mo_refs[i], vo_refs[i])

    whole = pl.BlockSpec(memory_space=pltpu.VMEM)
    outs = pl.pallas_call(
        body, name="adamw_small",
        in_specs=[whole] * (4 * n), out_specs=[whole] * (3 * n),
        out_shape=[jax.ShapeDtypeStruct(w.shape, F32) for w in ws] * 3,
    )(*ws, *gs, *ms, *vs)
    return outs[:n], outs[n:2 * n], outs[2 * n:]


def _position():
    return lax.axis_index("x"), lax.axis_index("y"), lax.axis_index("c")


def _chip_peers(x, y):
    return [(x, 1 - y), (1 - x, y), (1 - x, 1 - y)]


def _half_rows(ref, core, rows):
    h = rows // 2
    start = pl.multiple_of(core * h, 16)
    return ref.at[pl.ds(start, h), :] if len(ref.shape) == 2 else ref.at[:, pl.ds(start, h), :]


def _shard_half(ref, shard, core, rows):
    h = rows // 2
    return ref.at[shard, pl.ds(pl.multiple_of(core * h, 16), h), :]


ANY = pl.BlockSpec(memory_space=pl.ANY)


def _allgather_call(shards, whole):
    n, nw = len(shards), len(whole)
    outs_shape = [jax.ShapeDtypeStruct((N_CHIPS,) + s.shape, s.dtype) for s in shards + whole]

    def body(*refs):
        ins, outs = refs[:n + nw], refs[n + nw:2 * (n + nw)]
        send_sems, recv_sems, pass_send, pass_recv, own_send, own_recv = refs[2 * (n + nw):]
        x, y, c = _position()
        chip = 2 * x + y
        peers = _chip_peers(x, y)
        sent, local = [], []
        for i in range(n + nw):
            cp = pltpu.make_async_remote_copy(src_ref=ins[i], dst_ref=outs[i].at[chip], send_sem=own_send.at[i],
                                              recv_sem=own_recv.at[i], device_id=(x, y, 1 - c), device_id_type=MESH)
            cp.start()
            local.append(cp)
            rows = ins[i].shape[0]
            for k, (px, py) in enumerate(peers):
                if i < n:
                    src, dst = _half_rows(ins[i], c, rows), _shard_half(outs[i], chip, c, rows)
                else:
                    src, dst = ins[i], outs[i].at[chip]
                cp = pltpu.make_async_remote_copy(src_ref=src, dst_ref=dst, send_sem=send_sems.at[i, k],
                                                  recv_sem=recv_sems.at[i, k], device_id=(px, py, c), device_id_type=MESH)
                cp.start()
                sent.append(cp)
        passed = []
        for i in range(n + nw):
            rows = ins[i].shape[0]
            for k, (px, py) in enumerate(peers):
                landed = _shard_half(outs[i], 2 * px + py, c, rows) if i < n else outs[i].at[2 * px + py]
                pltpu.make_async_remote_copy(src_ref=landed, dst_ref=landed, send_sem=send_sems.at[i, k],
                                             recv_sem=recv_sems.at[i, k], device_id=(px, py, c), device_id_type=MESH).wait_recv()
                if i < n:
                    cp = pltpu.make_async_remote_copy(src_ref=landed, dst_ref=landed, send_sem=pass_send.at[i, k],
                                                      recv_sem=pass_recv.at[i, k], device_id=(x, y, 1 - c), device_id_type=MESH)
                    cp.start()
                    passed.append(cp)
        for cp in sent:
            cp.wait_send()
        for cp in passed:
            cp.wait()
        for cp in local:
            cp.wait()

    return pl.pallas_call(
        body, name="weight_allgather",
        in_specs=[ANY] * (n + nw), out_specs=[ANY] * (n + nw), out_shape=outs_shape,
        scratch_shapes=[pltpu.SemaphoreType.DMA((n + nw, 3)), pltpu.SemaphoreType.DMA((n + nw, 3)),
                        pltpu.SemaphoreType.DMA((n, 3)), pltpu.SemaphoreType.DMA((n, 3)),
                        pltpu.SemaphoreType.DMA((n + nw,)), pltpu.SemaphoreType.DMA((n + nw,))],
    )(*shards, *whole)


HBM_SPEC = pl.BlockSpec(memory_space=pltpu.HBM)
SEM_SPEC = pl.BlockSpec(memory_space=pltpu.SEMAPHORE)
DATAFLOW = pltpu.SideEffectType.DATAFLOW_SIDE_EFFECTING


def _in_hbm(a):
    return pltpu.with_memory_space_constraint(a, pltpu.HBM)


def _split_start(name, groups, after):
    spans, arrays = [], []
    for srcs, lands, _, _ in groups:
        spans.append((len(arrays), len(srcs), len(lands)))
        arrays += list(srcs) + list(lands)
    na, ng = len(arrays), len(groups)

    def body(*refs):
        sems, token = refs[na + 1:na + 1 + 2 * ng], refs[-1]
        for g, (_, _, _, copies) in enumerate(groups):
            off, ns, nl = spans[g]
            for src, dst, dev, idx in copies(refs[off:off + ns], refs[off + ns:off + ns + nl]):
                pltpu.make_async_remote_copy(src_ref=src, dst_ref=dst, send_sem=sems[2 * g].at[idx], recv_sem=sems[2 * g + 1].at[idx],
                                             device_id=dev, device_id_type=MESH).start()
        token[...] = jnp.zeros_like(token)

    outs = pl.pallas_call(
        body, name=name,
        in_specs=[HBM_SPEC] * na + [ANY],
        out_specs=[SEM_SPEC] * (2 * ng) + [HBM_SPEC] * na + [pl.BlockSpec(memory_space=pltpu.VMEM)],
        out_shape=[pltpu.SemaphoreType.DMA((n_sems,)) for _, _, n_sems, _ in groups for _ in range(2)]
        + [pltpu.HBM(a.shape, a.dtype) for a in arrays] + [jax.ShapeDtypeStruct((SUBLANES, LANES), F32)],
        input_output_aliases={i: 2 * ng + i for i in range(na)},
        compiler_params=pltpu.CompilerParams(has_side_effects=DATAFLOW),
    )(*[_in_hbm(a) for a in arrays], after)
    started = []
    for g, (off, ns, nl) in enumerate(spans):
        thru = outs[2 * ng + off:2 * ng + off + ns + nl]
        started.append(dict(send=outs[2 * g], recv=outs[2 * g + 1], srcs=list(thru[:ns]), lands=list(thru[ns:]),
                            tile=outs[-1], token=outs[-1][0, 0]))
    return started


def _split_wait(name, started, copies, after):
    n, m = len(started["srcs"]), len(started["lands"])

    def body(*refs):
        src_refs, land_refs = refs[:n], refs[n:n + m]
        send_sem, recv_sem = refs[n + m], refs[n + m + 1]
        for src, dst, dev, idx in copies(src_refs, land_refs):
            cp = pltpu.make_async_remote_copy(src_ref=src, dst_ref=dst, send_sem=send_sem.at[idx], recv_sem=recv_sem.at[idx],
                                              device_id=dev, device_id_type=MESH)
            cp.wait_send()
            cp.wait_recv()

    arrays = started["srcs"] + started["lands"]
    outs = pl.pallas_call(
        body, name=name,
        in_specs=[HBM_SPEC] * (n + m) + [SEM_SPEC, SEM_SPEC, ANY],
        out_specs=[HBM_SPEC] * (n + m),
        out_shape=[pltpu.HBM(a.shape, a.dtype) for a in arrays],
        input_output_aliases={i: i for i in range(n + m)},
        compiler_params=pltpu.CompilerParams(has_side_effects=DATAFLOW),
    )(*arrays, started["send"], started["recv"], after)
    return list(outs)


def _gather_copies(srcs, lands):
    x, y, c = _position()
    chip = 2 * x + y
    targets = [(px, py, c) for px, py in _chip_peers(x, y)] + [(x, y, 1 - c)]
    return [(s, l.at[chip], dev, len(targets) * i + k) for i, (s, l) in enumerate(zip(srcs, lands)) for k, dev in enumerate(targets)]


def _sibling_copies(srcs, lands):
    x, y, c = _position()
    return [(_half_rows(srcs[0], 1 - c, srcs[0].shape[1]), lands[0], (x, y, 1 - c), 0)]


def _exchange_copies(srcs, lands):
    x, y, c = _position()
    return [(srcs[0].at[2 * px + py], lands[0].at[k], (px, py, c), k) for k, (px, py) in enumerate(_chip_peers(x, y))]


def _pair_sum_call(grad, recv, core, name):
    _, h, B = recv.shape
    tr = _row_tile(h, B)

    def body(core_ref, g_ref, r_ref, o_ref, ob_ref):
        s = g_ref[...] + r_ref[...]
        o_ref[...] = s
        ob_ref[...] = s.astype(BF16)

    g_spec = pl.BlockSpec((None, tr, B), lambda q, i, core_ref: (q, core_ref[0] * (h // tr) + i, 0))
    spec = pl.BlockSpec((None, tr, B), lambda q, i, core_ref: (q, i, 0))
    return pl.pallas_call(
        body, name=name,
        grid_spec=pltpu.PrefetchScalarGridSpec(num_scalar_prefetch=1, grid=(N_CHIPS, h // tr), in_specs=[g_spec, spec],
                                               out_specs=[spec, spec]),
        out_shape=[jax.ShapeDtypeStruct(recv.shape, F32), jax.ShapeDtypeStruct(recv.shape, BF16)],
        compiler_params=_params(("parallel", "parallel")),
    )(core, grad, recv)


def _chip_sum_call(partial, recv, chip_core, name):
    _, h, B = recv.shape
    tr = _row_tile(h, B)

    def body(cc_ref, p_ref, r_ref, o_ref):
        o_ref[...] = ((p_ref[...] + r_ref[0].astype(F32)) + r_ref[1].astype(F32)) + r_ref[2].astype(F32)

    return pl.pallas_call(
        body, name=name,
        grid_spec=pltpu.PrefetchScalarGridSpec(
            num_scalar_prefetch=1, grid=(h // tr,),
            in_specs=[pl.BlockSpec((None, tr, B), lambda i, cc_ref: (cc_ref[0], i, 0)),
                      pl.BlockSpec((3, tr, B), lambda i, cc_ref: (0, i, 0))],
            out_specs=pl.BlockSpec((tr, B), lambda i, cc_ref: (cc_ref[1] * (h // tr) + i, 0))),
        out_shape=jax.ShapeDtypeStruct((2 * h, B), F32),
        compiler_params=_params(("parallel",)),
    )(chip_core, partial, recv)


def _sibling_assemble_call(shards, name="grad_sibling_assemble"):
    n = len(shards)

    def body(*refs):
        ins, outs = refs[:n], refs[n:2 * n]
        send_sems, recv_sems = refs[2 * n:]
        x, y, c = _position()
        copies = []
        for i in range(n):
            rows = shards[i].shape[0]
            cp = pltpu.make_async_remote_copy(src_ref=_half_rows(ins[i], c, rows), dst_ref=_half_rows(outs[i], c, rows),
                                              send_sem=send_sems.at[i], recv_sem=recv_sems.at[i],
                                              device_id=(x, y, 1 - c), device_id_type=MESH)
            cp.start()
            copies.append(cp)
        for cp in copies:
            cp.wait()

    return pl.pallas_call(
        body, name=name,
        in_specs=[ANY] * n, out_specs=[ANY] * n,
        out_shape=[jax.ShapeDtypeStruct(s.shape, F32) for s in shards],
        input_output_aliases={i: i for i in range(n)},
        scratch_shapes=[pltpu.SemaphoreType.DMA((n,)), pltpu.SemaphoreType.DMA((n,))],
    )(*shards)


N_DEVICES = 8


def _allsum_copies(srcs, lands):
    x, y, c = _position()
    me = 4 * x + 2 * y + c
    out = []
    for k in range(1, N_DEVICES):
        peer = (1 - x if k & 4 else x, 1 - y if k & 2 else y, 1 - c if k & 1 else c)
        out.append((srcs[0], lands[0].at[me], peer, k - 1))
    return out


def _ordered_sum_call(mine, landed, me):
    rows = mine.shape[0]

    def body(me_ref, x_ref, l_ref, o_ref):
        acc = jnp.where(me_ref[0] == 0, x_ref[...], l_ref[0])
        for d in range(1, N_DEVICES):
            acc = acc + jnp.where(me_ref[0] == d, x_ref[...], l_ref[d])
        o_ref[...] = acc

    return pl.pallas_call(
        body, name="small_grad_sum",
        in_specs=[pl.BlockSpec(memory_space=pltpu.SMEM), pl.BlockSpec(memory_space=pltpu.VMEM), pl.BlockSpec(memory_space=pltpu.VMEM)],
        out_specs=pl.BlockSpec(memory_space=pltpu.VMEM),
        out_shape=jax.ShapeDtypeStruct((rows, LANES), F32),
    )(me, mine, landed)


def _pack(arrays):
    flat = jnp.concatenate([a.reshape(-1).astype(F32) for a in arrays])
    rows = -(-flat.shape[0] // LANES)
    rows = -(-rows // SUBLANES) * SUBLANES
    flat = jnp.pad(flat, (0, rows * LANES - flat.shape[0]))
    return flat.reshape(rows, LANES)


def _unpack(packed, shapes):
    flat = packed.reshape(-1)
    out, off = [], 0
    for shp in shapes:
        size = int(np.prod(shp))
        out.append(flat[off:off + size].reshape(shp))
        off += size
    return out


def _local_step(xs, target, P, late_weights, on_grad):
    S, D = xs.shape
    qkv_width = 3 * N_HEADS * HEAD_DIM
    glu_col0, gate_col0 = qkv_width, qkv_width + 2 * D
    shard_major = lambda g: g.reshape(N_CHIPS, g.shape[0] // N_CHIPS, g.shape[1])

    h1 = _rms_fwd_call(xs, P["norm_mix_pre"])
    proj = _matmul(h1, P["w_in"], "nn", "proj_in")
    buckets = _bucket_tables()
    bias = _bias_table_call(P["rel_bias"], buckets)
    parts = []
    for g in range(N_GROUPS):
        parts += _attn_fwd_call(proj, bias, g)
    a, a_bf, lse = _attn_merge_call(parts)
    P = dict(P, **late_weights("mix", a_bf))
    y_a = _matmul(a_bf, P["w_attn_out"], "nn", "attn_out")
    c1 = _conv_fwd_call(proj, glu_col0, P["conv_dw_w"], P["conv_dw_b"])
    cact = _ln_silu_call(c1, P["conv_ln_g"], P["conv_ln_b"])
    y_c = _matmul(cact, P["conv_pw_w"], "nn", "conv_pw")
    mixed = _mix_call(proj, gate_col0, P["b_gate"], y_a, y_c)
    out = _matmul(mixed, P["w_out"], "nn", "mix_out")
    x1, h2 = _res1_call(xs, out, P["norm_mix_post"], P["norm_ffn_pre"])
    P = dict(P, **late_weights("ffn", h2))
    u = _matmul(h2, P["w_up"], "nn", "ffn_up")
    f = _ffn_fwd_call(u, P["ffn_conv_w"], P["ffn_conv_b"])
    yff = _matmul(f, P["w_down"], "nn", "ffn_down")
    loss_tile, dx2 = _loss_call(yff, x1, P["norm_ffn_post"], target)

    G = {}
    dyff, G["norm_ffn_post"] = _rms_bwd_call(yff, P["norm_ffn_post"], dx2, "rms_ffn_post_bwd")
    zero = on_grad("w_down", shard_major(_matmul(f, dyff, "tn", "ffn_down_dw")))
    df = _matmul(dyff, P["w_down"], "nt", "ffn_down_dx")
    du, dwg, dwv, dbg, dbv = _ffn_bwd_call(u, P["ffn_conv_w"], P["ffn_conv_b"] + zero, df)
    G["ffn_conv_w"] = jnp.concatenate([dwg[:FFN_CONV_WIDTH], dwv[:FFN_CONV_WIDTH]], axis=1)
    G["ffn_conv_b"] = jnp.concatenate([dbg, dbv], axis=1)
    zero = on_grad("w_up", _matmul(h2, du, "tn", "ffn_up_dw", out_shards=True))
    dh2 = _matmul(du, P["w_up"], "nt", "ffn_up_dx")
    dx1, dout, G["norm_ffn_pre"], G["norm_mix_post"] = _mid_bwd_call(x1, P["norm_ffn_pre"] + zero, dh2, dx2, out, P["norm_mix_post"])
    zero = on_grad("w_out", shard_major(_matmul(mixed, dout, "tn", "mix_out_dw")))
    dmixed = _matmul(dout, P["w_out"], "nt", "mix_out_dx")
    dya, dyc, dga, dgc, dba, dbc = _mix_bwd_call(dmixed, proj, gate_col0, P["b_gate"] + zero, y_a, y_c)
    G["b_gate"] = jnp.concatenate([dba, dbc], axis=1)
    zero = on_grad("w_attn_out", _matmul(a_bf, dya, "tn", "attn_out_dw", out_shards=True))
    zero = zero + on_grad("conv_pw_w", shard_major(_matmul(cact, dyc, "tn", "conv_pw_dw")))
    da = _matmul(dya, P["w_attn_out"], "nt", "attn_out_dx")
    dcact = _matmul(dyc, P["conv_pw_w"], "nt", "conv_pw_dx")
    dc1, G["conv_ln_g"], G["conv_ln_b"] = _ln_silu_bwd_call(c1, P["conv_ln_g"] + zero, P["conv_ln_b"], dcact)
    dval, dgate, dw_dw, G["conv_dw_b"] = _conv_bwd_call(proj, glu_col0, P["conv_dw_w"], dc1)
    G["conv_dw_w"] = dw_dw[:CONV_WIDTH]
    delta = _attn_delta_call(a, da)
    dqs, dks, dvs, dbs = [], [], [], []
    for g in range(N_GROUPS):
        dq, dk, dv, db = _attn_bwd_call(proj, bias, da, lse, delta, g)
        dqs.append(dq)
        dks.append(dk)
        dvs.append(dv)
        dbs.append(db)
    G["rel_bias"] = _bias_grad_call(jnp.concatenate(dbs, axis=0), buckets)
    dproj = _dproj_call(dqs + dks + dvs, [dval, dgate, dga, dgc])
    zero = on_grad("w_in", _matmul(h1, dproj, "tn", "proj_in_dw", out_shards=True, tm=512))
    dh1 = _matmul(dproj, P["w_in"], "nt", "proj_in_dx")
    grad_x, G["norm_mix_pre"] = _in_bwd_call(xs, P["norm_mix_pre"] + zero, dh1, dx1)
    return loss_tile, grad_x, G


def kernel(x, w_in, b_gate, rel_bias, w_attn_out, conv_dw_w, conv_dw_b, conv_ln_g, conv_ln_b, conv_pw_w, w_out, norm_mix_pre, norm_mix_post, norm_ffn_pre, norm_ffn_post, w_up, ffn_conv_w, ffn_conv_b, w_down, loss_target, m_w_in, m_b_gate, m_rel_bias, m_w_attn_out, m_conv_dw_w, m_conv_dw_b, m_conv_ln_g, m_conv_ln_b, m_conv_pw_w, m_w_out, m_norm_mix_pre, m_norm_mix_post, m_norm_ffn_pre, m_norm_ffn_post, m_w_up, m_ffn_conv_w, m_ffn_conv_b, m_w_down, v_w_in, v_b_gate, v_rel_bias, v_w_attn_out, v_conv_dw_w, v_conv_dw_b, v_conv_ln_g, v_conv_ln_b, v_conv_pw_w, v_w_out, v_norm_mix_pre, v_norm_mix_post, v_norm_ffn_pre, v_norm_ffn_post, v_w_up, v_ffn_conv_w, v_ffn_conv_b, v_w_down):
    weights = dict(w_in=w_in, b_gate=b_gate, rel_bias=rel_bias, w_attn_out=w_attn_out, conv_dw_w=conv_dw_w, conv_dw_b=conv_dw_b,
                   conv_ln_g=conv_ln_g, conv_ln_b=conv_ln_b, conv_pw_w=conv_pw_w, w_out=w_out, norm_mix_pre=norm_mix_pre,
                   norm_mix_post=norm_mix_post, norm_ffn_pre=norm_ffn_pre, norm_ffn_post=norm_ffn_post, w_up=w_up,
                   ffn_conv_w=ffn_conv_w, ffn_conv_b=ffn_conv_b, w_down=w_down)
    m_in = dict(w_in=m_w_in, b_gate=m_b_gate, rel_bias=m_rel_bias, w_attn_out=m_w_attn_out, conv_dw_w=m_conv_dw_w,
                conv_dw_b=m_conv_dw_b, conv_ln_g=m_conv_ln_g, conv_ln_b=m_conv_ln_b, conv_pw_w=m_conv_pw_w, w_out=m_w_out,
                norm_mix_pre=m_norm_mix_pre, norm_mix_post=m_norm_mix_post, norm_ffn_pre=m_norm_ffn_pre,
                norm_ffn_post=m_norm_ffn_post, w_up=m_w_up, ffn_conv_w=m_ffn_conv_w, ffn_conv_b=m_ffn_conv_b, w_down=m_w_down)
    v_in = dict(w_in=v_w_in, b_gate=v_b_gate, rel_bias=v_rel_bias, w_attn_out=v_w_attn_out, conv_dw_w=v_conv_dw_w,
                conv_dw_b=v_conv_dw_b, conv_ln_g=v_conv_ln_g, conv_ln_b=v_conv_ln_b, conv_pw_w=v_conv_pw_w, w_out=v_w_out,
                norm_mix_pre=v_norm_mix_pre, norm_mix_post=v_norm_mix_post, norm_ffn_pre=v_norm_ffn_pre,
                norm_ffn_post=v_norm_ffn_post, w_up=v_w_up, ffn_conv_w=v_ffn_conv_w, ffn_conv_b=v_ffn_conv_b, w_down=v_w_down)
    names = list(weights)
    xi, yi, ci = _position()
    chip = 2 * xi + yi
    core_arr = jnp.reshape(ci, (1,)).astype(jnp.int32)

    xs = x[0]
    target = loss_target[0]
    S, D = xs.shape

    big = ["w_in", "w_attn_out", "conv_pw_w", "w_out", "w_up", "w_down"]
    row_sharded = ("conv_pw_w", "w_out", "w_down")
    bf16_shard = {k: weights[k][0].astype(BF16) for k in big}
    natural = lambda k, g: g.reshape(-1, g.shape[2]) if k in row_sharded else g
    w_in_full, dw4, fc4 = _allgather_call([bf16_shard["w_in"]], [conv_dw_w[0], ffn_conv_w[0]])
    late_sets = dict(mix=["w_attn_out", "conv_pw_w", "w_out"], ffn=["w_up", "w_down"])
    late_groups = []
    for keys in late_sets.values():
        srcs = [bf16_shard[k] for k in keys]
        late_groups.append((srcs, [lax.empty((N_CHIPS,) + s.shape, BF16) for s in srcs], 4 * len(keys), _gather_copies))
    started = dict(zip(late_sets, _split_start("gather_late_start", late_groups, w_in_full)))
    launched = started["mix"]["token"]

    def late_weights(tag, after):
        landed = _split_wait(f"gather_{tag}_wait", started[tag], _gather_copies, after)[len(late_sets[tag]):]
        return {k: natural(k, g) for k, g in zip(late_sets[tag], landed)}

    chip_core = jnp.stack([chip, ci]).astype(jnp.int32)
    exchanging, pending = {}, {}

    def launch(tag, g3, after):
        keys, groups, partial = [], [], {}
        for k in list(exchanging):
            gk, r1 = _split_wait(f"sibling_exchange_wait_{k}", exchanging.pop(k), _sibling_copies, after)
            partial[k], s16 = _pair_sum_call(gk, r1, core_arr, f"pair_sum_{k}")
            keys.append(k)
            groups.append(([s16], [lax.empty((3,) + s16.shape[1:], BF16)], 3, _exchange_copies))
        if g3 is not None:
            groups.append(([g3], [lax.empty((N_CHIPS, g3.shape[1] // 2, g3.shape[2]), F32)], 1, _sibling_copies))
        begun = _split_start(f"grad_exchange_start_{tag}", groups, core_arr)
        for k, st in zip(keys, begun):
            pending[k] = (partial[k], st)
        if g3 is not None:
            exchanging[tag] = begun[-1]
        return begun[0]["token"]

    def on_grad(k, g3):
        return launch(k, g3, g3[0, :SUBLANES, :LANES])

    def pair_up(after):
        return launch("last", None, after)

    def finish(keys, after, tag):
        halves = []
        for k in keys:
            s32, st = pending[k]
            recv2 = _split_wait(f"chip_exchange_wait_{k}", st, _exchange_copies, after)[1]
            halves.append(_chip_sum_call(s32, recv2, chip_core, f"chip_sum_{k}"))
        return dict(zip(keys, _sibling_assemble_call(halves, f"grad_sibling_assemble_{tag}")))

    P = dict(w_in=w_in_full, conv_dw_w=jnp.concatenate(list(dw4), axis=1), ffn_conv_w=jnp.concatenate(list(fc4), axis=1),
             b_gate=b_gate, rel_bias=rel_bias, conv_dw_b=conv_dw_b, conv_ln_g=conv_ln_g, conv_ln_b=conv_ln_b,
             norm_mix_pre=norm_mix_pre + launched, norm_mix_post=norm_mix_post, norm_ffn_pre=norm_ffn_pre,
             norm_ffn_post=norm_ffn_post, ffn_conv_b=ffn_conv_b)
    loss_tile, grad_x, G = _local_step(xs, target, P, late_weights, on_grad)

    small = [k for k in names if k not in big]
    packed = _pack([loss_tile[:1]] + [G[k] for k in small])
    (allsum,) = _split_start("small_grad_allsum_start",
                             [([packed], [jnp.zeros((N_DEVICES,) + packed.shape, F32)], N_DEVICES - 1, _allsum_copies)], core_arr)
    launched = pair_up(allsum["tile"] + grad_x[:SUBLANES, :LANES])

    reduced, grads, deltas, new_m, new_v = {}, {}, {}, {}, {}

    def update(keys):
        for k in keys:
            d, mn, vn = _adamw_call(weights[k][0], reduced[k], m_in[k][0], v_in[k][0], f"adamw_{k}")
            grads[k], deltas[k], new_m[k], new_v[k] = reduced[k][None], d[None], mn[None], vn[None]

    others = [k for k in big if k != "w_in"]
    reduced.update(finish(others, allsum["tile"] + launched, "others"))
    update(others)

    me = jnp.reshape(4 * xi + 2 * yi + ci, (1,)).astype(jnp.int32)
    mine, landed = _split_wait("small_grad_allsum_wait", allsum, _allsum_copies, deltas["w_up"])
    summed_block = _ordered_sum_call(mine, landed, me)
    loss_row, *summed = _unpack(summed_block, [(1, LANES)] + [G[k].shape for k in small])
    loss = loss_row[0, 0]
    for k, gsum in zip(small, summed):
        if k in ("conv_dw_w", "ffn_conv_w"):
            cols = weights[k].shape[2]
            reduced[k] = lax.dynamic_slice_in_dim(gsum, chip * cols, cols, axis=1)
        else:
            reduced[k] = gsum
    flat2 = lambda t: t.reshape(-1, t.shape[-1]) if t.ndim == 3 else t
    ds, mns, vns = _adamw_small_call([flat2(weights[k]) for k in small], [reduced[k] for k in small],
                                     [flat2(m_in[k]) for k in small], [flat2(v_in[k]) for k in small])
    for k, dk_, mk, vk in zip(small, ds, mns, vns):
        shape = weights[k].shape
        grads[k], deltas[k], new_m[k], new_v[k] = reduced[k].reshape(shape), dk_.reshape(shape), mk.reshape(shape), vk.reshape(shape)

    reduced.update(finish(["w_in"], deltas["norm_mix_pre"], "w_in"))
    update(["w_in"])

    return (loss, grad_x[None], *[grads[k] for k in names], *[deltas[k] for k in names],
            *[new_m[k] for k in names], *[new_v[k] for k in names])
```

```python
import functools
import math

import jax
import jax.numpy as jnp
import numpy as np
from jax import lax
from jax.experimental import pallas as pl
from jax.experimental.pallas import tpu as pltpu

F32 = jnp.float32
BF16 = jnp.bfloat16
MESH = pl.DeviceIdType.MESH

HEAD_DIM = 128
HEADS_PER_GROUP = 4
DILATED_PATTERNS = ((128, 1), (512, 4), (2048, 16))
N_GROUPS = 3
N_HEADS = N_GROUPS * HEADS_PER_GROUP
SPAN = 128
GROUP_WIDTH = HEADS_PER_GROUP * HEAD_DIM
CONV_WIDTH = 31
FFN_CONV_WIDTH = 3
N_BUCKETS = 32
MAX_DISTANCE = 2048
RMS_EPS = 1e-6
LN_EPS = 1e-5
NEG_INF = -1e30
ADAM_LR = 0.001
ADAM_B1 = 0.9
ADAM_B2 = 0.999
ADAM_EPS = 1e-08
ADAM_WD = 0.01
ADAM_STEP = 10

LANES = 128
SUBLANES = 8
ROW_TILE = 256
GATE_ROWS, GATE_COLS = 512, 512
TIME_BLOCK = 128
CONV_PAD = 32
FFN_PAD = 8
VMEM_LIMIT = 56 << 20


def _params(sem=None, vmem=None):
    kw = {}
    if sem is not None:
        kw["dimension_semantics"] = sem
    if vmem is not None:
        kw["vmem_limit_bytes"] = vmem
    return pltpu.CompilerParams(**kw)


def _pick(n, cands):
    for c in cands:
        if n % c == 0:
            return c
    return n


ELEMENTWISE_TILE_BYTES = 3 << 19


def _row_tile(rows, cols):
    for align in (16, SUBLANES):
        fits = [t for t in range(align, rows + 1, align) if rows % t == 0 and t * cols * 4 <= ELEMENTWISE_TILE_BYTES]
        if fits:
            return max(fits)
    return SUBLANES


N_CHIPS = 4
M_TILES = (1024, 1408, 512, 256, 128)
N_TILES = (1024, 512, 1408, 256, 128)
K_TILES = (2176, 2048, 1408, 1024, 512, 256, 128)


def _matmul(a, b, mode, name, out_shards=False, tm=None):
    assert a.dtype == BF16 and b.dtype == BF16, (name, a.dtype, b.dtype)
    b3 = b.ndim == 3
    tn = tk = None
    halves = None
    if mode == "nn":
        M, K = a.shape
        N = b.shape[-1] * (N_CHIPS if b3 else 1)
        tn = b.shape[-1] if b3 else None
    elif mode == "nt":
        if a.ndim == 3:
            halves = a.shape[2]
        M, K = a.shape[-2], a.shape[-1] * (a.shape[0] if a.ndim == 3 else 1)
        N = b.shape[-2]
        tk = b.shape[-1] if b3 else None
    else:
        if b3:
            halves = b.shape[2]
        K, M = a.shape
        N = b.shape[-1] * (b.shape[0] if b3 else 1)
        tn = N // N_CHIPS if out_shards else None
    tm = tm or _pick(M, M_TILES)
    tn = tn or _pick(N, N_TILES)
    tk = tk or _pick(K, K_TILES)
    nk = K // tk
    dn = {"nn": (((1,), (0,)), ((), ())), "nt": (((1,), (1,)), ((), ())), "tn": (((0,), (0,)), ((), ()))}[mode]

    def body(a_ref, b_ref, o_ref):
        if nk == 1:
            o_ref[...] = lax.dot_general(a_ref[...], b_ref[...], dn, preferred_element_type=F32)
        else:
            @pl.when(pl.program_id(2) == 0)
            def _():
                o_ref[...] = jnp.zeros_like(o_ref)

            o_ref[...] += lax.dot_general(a_ref[...], b_ref[...], dn, preferred_element_type=F32)

    if mode == "tn":
        a_spec = pl.BlockSpec((tk, tm), lambda i, j, k: (k, i))
    elif halves:
        per = halves // tk
        a_spec = pl.BlockSpec((None, tm, tk), lambda i, j, k: (k // per, i, k % per))
    else:
        a_spec = pl.BlockSpec((tm, tk), lambda i, j, k: (i, k))
    if mode == "nn":
        b_spec = pl.BlockSpec((None, tk, tn), lambda i, j, k: (j, k, 0)) if b3 else pl.BlockSpec((tk, tn), lambda i, j, k: (k, j))
    elif mode == "nt":
        b_spec = pl.BlockSpec((None, tn, tk), lambda i, j, k: (k, j, 0)) if b3 else pl.BlockSpec((tn, tk), lambda i, j, k: (j, k))
    elif halves:
        per = halves // tn
        b_spec = pl.BlockSpec((None, tk, tn), lambda i, j, k: (j // per, k, j % per))
    else:
        b_spec = pl.BlockSpec((tk, tn), lambda i, j, k: (k, j))
    if out_shards:
        out_spec = pl.BlockSpec((None, tm, tn), lambda i, j, k: (j, i, 0))
        out_shape = jax.ShapeDtypeStruct((N_CHIPS, M, tn), F32)
    else:
        out_spec = pl.BlockSpec((tm, tn), lambda i, j, k: (i, j))
        out_shape = jax.ShapeDtypeStruct((M, N), F32)
    return pl.pallas_call(
        body, name=name, grid=(M // tm, N // tn, nk),
        in_specs=[a_spec, b_spec], out_specs=out_spec, out_shape=out_shape,
        compiler_params=_params(("parallel", "parallel", "arbitrary"), VMEM_LIMIT),
    )(a, b)


def _rms(x, g):
    r = lax.rsqrt(jnp.mean(x * x, axis=-1, keepdims=True) + RMS_EPS)
    return x * r * g


def _rms_bwd(x, g, dy):
    r = lax.rsqrt(jnp.mean(x * x, axis=-1, keepdims=True) + RMS_EPS)
    n = x * r
    dn = dy * g
    dx = r * (dn - n * jnp.mean(dn * n, axis=-1, keepdims=True))
    return dx, jnp.sum(dy * n, axis=0, keepdims=True)


def _sigmoid(x):
    return 1.0 / (1.0 + jnp.exp(-x))


_GELU_C = math.sqrt(2.0 / math.pi)


def _gelu(x):
    return 0.5 * x * (1.0 + jnp.tanh(_GELU_C * (x + 0.044715 * x * x * x)))


def _gelu_and_grad(x):
    x2 = x * x
    t = jnp.tanh(_GELU_C * x * (1.0 + 0.044715 * x2))
    half = 0.5 * (1.0 + t)
    return x * half, half + (0.5 * _GELU_C) * x * (1.0 - t * t) * (1.0 + (3.0 * 0.044715) * x2)


def _row_spec(width, col_block=0):
    return pl.BlockSpec((ROW_TILE, width), lambda i: (i, col_block))


def _vec_spec(width, col_block=0):
    return pl.BlockSpec((1, width), lambda i: (0, col_block))


def _accumulate(ref, part):
    @pl.when(pl.program_id(0) == 0)
    def _():
        ref[...] = part

    @pl.when(pl.program_id(0) > 0)
    def _():
        ref[...] += part


def _rms_fwd_call(x, g):
    S, D = x.shape

    def body(x_ref, g_ref, h_ref):
        h_ref[...] = _rms(x_ref[...], g_ref[...]).astype(BF16)

    return pl.pallas_call(
        body, name="rms_mix_pre", grid=(S // ROW_TILE,),
        in_specs=[_row_spec(D), _vec_spec(D)], out_specs=_row_spec(D),
        out_shape=jax.ShapeDtypeStruct((S, D), BF16),
        compiler_params=_params(("parallel",)),
    )(x, g)


def _ln_silu_call(c1, g, b):
    S, C = c1.shape

    def body(c_ref, g_ref, b_ref, o_ref):
        xv = c_ref[...]
        mu = jnp.mean(xv, axis=-1, keepdims=True)
        xc = xv - mu
        var = jnp.mean(xc * xc, axis=-1, keepdims=True)
        z = xc * lax.rsqrt(var + LN_EPS) * g_ref[...] + b_ref[...]
        o_ref[...] = (z * _sigmoid(z)).astype(BF16)

    return pl.pallas_call(
        body, name="conv_ln_silu", grid=(S // ROW_TILE,),
        in_specs=[_row_spec(C), _vec_spec(C), _vec_spec(C)], out_specs=_row_spec(C),
        out_shape=jax.ShapeDtypeStruct((S, C), BF16),
        compiler_params=_params(("parallel",)),
    )(c1, g, b)


def _ln_silu_bwd_call(c1, g, b, dc):
    S, C = c1.shape

    def body(c_ref, g_ref, b_ref, dc_ref, dx_ref, dg_ref, db_ref):
        xv = c_ref[...]
        mu = jnp.mean(xv, axis=-1, keepdims=True)
        xc = xv - mu
        rs = lax.rsqrt(jnp.mean(xc * xc, axis=-1, keepdims=True) + LN_EPS)
        xh = xc * rs
        z = xh * g_ref[...] + b_ref[...]
        sg = _sigmoid(z)
        dz = dc_ref[...] * (sg * (1.0 + z * (1.0 - sg)))
        dxh = dz * g_ref[...]
        dx_ref[...] = rs * (dxh - jnp.mean(dxh, axis=-1, keepdims=True) - xh * jnp.mean(dxh * xh, axis=-1, keepdims=True))
        _accumulate(dg_ref, jnp.sum(dz * xh, axis=0, keepdims=True))
        _accumulate(db_ref, jnp.sum(dz, axis=0, keepdims=True))

    return pl.pallas_call(
        body, name="conv_ln_silu_bwd", grid=(S // ROW_TILE,),
        in_specs=[_row_spec(C), _vec_spec(C), _vec_spec(C), _row_spec(C)],
        out_specs=[_row_spec(C), _vec_spec(C), _vec_spec(C)],
        out_shape=[jax.ShapeDtypeStruct((S, C), F32), jax.ShapeDtypeStruct((1, C), F32), jax.ShapeDtypeStruct((1, C), F32)],
        compiler_params=_params(("arbitrary",)),
    )(c1, g, b, dc)


def _mix_call(proj, gate_col0, b_gate, y_a, y_c):
    S, D = y_a.shape
    w = GATE_COLS
    nc = D // w
    ga0, gc0 = gate_col0 // w, (gate_col0 + D) // w

    def body(ga_ref, gc_ref, ba_ref, bc_ref, ya_ref, yc_ref, o_ref):
        o_ref[...] = (_sigmoid(ga_ref[...] + ba_ref[...]) * ya_ref[...]
                      + _sigmoid(gc_ref[...] + bc_ref[...]) * yc_ref[...]).astype(BF16)

    tile = lambda off: pl.BlockSpec((GATE_ROWS, w), lambda i, j: (i, off + j))
    vec = lambda off: pl.BlockSpec((1, w), lambda i, j: (0, off + j))
    return pl.pallas_call(
        body, name="gate_mix", grid=(S // GATE_ROWS, nc),
        in_specs=[tile(ga0), tile(gc0), vec(0), vec(nc), tile(0), tile(0)],
        out_specs=tile(0), out_shape=jax.ShapeDtypeStruct((S, D), BF16),
        compiler_params=_params(("parallel", "parallel")),
    )(proj, proj, b_gate, b_gate, y_a, y_c)


def _mix_bwd_call(dmixed, proj, gate_col0, b_gate, y_a, y_c):
    S, D = y_a.shape
    w = GATE_COLS
    nc = D // w
    ga0, gc0 = gate_col0 // w, (gate_col0 + D) // w

    def body(dm_ref, ga_ref, gc_ref, ba_ref, bc_ref, ya_ref, yc_ref, dya_ref, dyc_ref, dga_ref, dgc_ref, dba_ref, dbc_ref):
        dm = dm_ref[...]
        sa = _sigmoid(ga_ref[...] + ba_ref[...])
        sc = _sigmoid(gc_ref[...] + bc_ref[...])
        dya_ref[...] = (dm * sa).astype(BF16)
        dyc_ref[...] = (dm * sc).astype(BF16)
        dga = dm * ya_ref[...] * sa * (1.0 - sa)
        dgc = dm * yc_ref[...] * sc * (1.0 - sc)
        dga_ref[...] = dga.astype(BF16)
        dgc_ref[...] = dgc.astype(BF16)
        pa = jnp.sum(dga, axis=0, keepdims=True)
        pc = jnp.sum(dgc, axis=0, keepdims=True)

        @pl.when(pl.program_id(1) == 0)
        def _():
            dba_ref[...] = pa
            dbc_ref[...] = pc

        @pl.when(pl.program_id(1) > 0)
        def _():
            dba_ref[...] += pa
            dbc_ref[...] += pc

    tile = lambda off: pl.BlockSpec((GATE_ROWS, w), lambda j, i: (i, off + j))
    vec = lambda off: pl.BlockSpec((1, w), lambda j, i: (0, off + j))
    return pl.pallas_call(
        body, name="gate_mix_bwd", grid=(nc, S // GATE_ROWS),
        in_specs=[tile(0), tile(ga0), tile(gc0), vec(0), vec(nc), tile(0), tile(0)],
        out_specs=[tile(0), tile(0), tile(0), tile(0), vec(0), vec(0)],
        out_shape=[jax.ShapeDtypeStruct((S, D), BF16)] * 4 + [
                   jax.ShapeDtypeStruct((1, D), F32), jax.ShapeDtypeStruct((1, D), F32)],
        compiler_params=_params(("parallel", "arbitrary")),
    )(dmixed, proj, proj, b_gate, b_gate, y_a, y_c)


def _res1_call(x, out, g_post, g_pre):
    S, D = x.shape

    def body(x_ref, o_ref, gp_ref, gq_ref, x1_ref, h2_ref):
        x1 = x_ref[...] + _rms(o_ref[...], gp_ref[...])
        x1_ref[...] = x1
        h2_ref[...] = _rms(x1, gq_ref[...]).astype(BF16)

    return pl.pallas_call(
        body, name="residual_mix", grid=(S // ROW_TILE,),
        in_specs=[_row_spec(D), _row_spec(D), _vec_spec(D), _vec_spec(D)],
        out_specs=[_row_spec(D), _row_spec(D)],
        out_shape=[jax.ShapeDtypeStruct((S, D), F32), jax.ShapeDtypeStruct((S, D), BF16)],
        compiler_params=_params(("parallel",)),
    )(x, out, g_post, g_pre)


def _loss_call(y, x1, g_post, target):
    S, D = y.shape

    def body(y_ref, x1_ref, g_ref, t_ref, loss_ref, dx_ref, dy_ref, dg_ref):
        yv, gv = y_ref[...], g_ref[...]
        err = x1_ref[...] + _rms(yv, gv) - t_ref[...]
        dx2 = err * (1.0 / D)
        dx_ref[...] = dx2
        dy, dg = _rms_bwd(yv, gv, dx2)
        dy_ref[...] = dy.astype(BF16)
        _accumulate(dg_ref, dg)
        part = 0.5 * jnp.sum(jnp.mean(err * err, axis=-1, keepdims=True), axis=0, keepdims=True)
        _accumulate(loss_ref, jnp.broadcast_to(part, (SUBLANES, LANES)))

    return pl.pallas_call(
        body, name="residual_ffn_loss", grid=(S // ROW_TILE,),
        in_specs=[_row_spec(D), _row_spec(D), _vec_spec(D), _row_spec(D)],
        out_specs=[pl.BlockSpec((SUBLANES, LANES), lambda i: (0, 0)), _row_spec(D), _row_spec(D), _vec_spec(D)],
        out_shape=[jax.ShapeDtypeStruct((SUBLANES, LANES), F32), jax.ShapeDtypeStruct((S, D), F32),
                   jax.ShapeDtypeStruct((S, D), BF16), jax.ShapeDtypeStruct((1, D), F32)],
        compiler_params=_params(("arbitrary",)),
    )(y, x1, g_post, target)


def _mid_bwd_call(x1, g_pre, dh2, dx2, out, g_post):
    S, D = x1.shape

    def body(x1_ref, gq_ref, dh_ref, dx2_ref, o_ref, gp_ref, dx1_ref, do_ref, dgq_ref, dgp_ref):
        d, dgq = _rms_bwd(x1_ref[...], gq_ref[...], dh_ref[...])
        dx1 = dx2_ref[...] + d
        dx1_ref[...] = dx1
        do, dgp = _rms_bwd(o_ref[...], gp_ref[...], dx1)
        do_ref[...] = do.astype(BF16)
        _accumulate(dgq_ref, dgq)
        _accumulate(dgp_ref, dgp)

    return pl.pallas_call(
        body, name="residual_mix_bwd", grid=(S // ROW_TILE,),
        in_specs=[_row_spec(D), _vec_spec(D), _row_spec(D), _row_spec(D), _row_spec(D), _vec_spec(D)],
        out_specs=[_row_spec(D), _row_spec(D), _vec_spec(D), _vec_spec(D)],
        out_shape=[jax.ShapeDtypeStruct((S, D), F32), jax.ShapeDtypeStruct((S, D), BF16)] + [jax.ShapeDtypeStruct((1, D), F32)] * 2,
        compiler_params=_params(("arbitrary",)),
    )(x1, g_pre, dh2, dx2, out, g_post)


def _in_bwd_call(x, g, dh1, dx1):
    S, D = x.shape

    def body(x_ref, g_ref, dh_ref, dx1_ref, gx_ref, dg_ref):
        d, dg = _rms_bwd(x_ref[...], g_ref[...], dh_ref[...])
        gx_ref[...] = dx1_ref[...] + d
        _accumulate(dg_ref, dg)

    return pl.pallas_call(
        body, name="rms_mix_pre_bwd", grid=(S // ROW_TILE,),
        in_specs=[_row_spec(D), _vec_spec(D), _row_spec(D), _row_spec(D)],
        out_specs=[_row_spec(D), _vec_spec(D)],
        out_shape=[jax.ShapeDtypeStruct((S, D), F32), jax.ShapeDtypeStruct((1, D), F32)],
        compiler_params=_params(("arbitrary",)),
    )(x, g, dh1, dx1)


def _bucket_table(dilation):
    qi = np.arange(SPAN)[:, None]
    ki = np.arange(2 * SPAN)[None, :]
    dist = np.maximum(qi + SPAN - ki, 0) * dilation
    max_exact = N_BUCKETS // 2
    d = np.maximum(dist, 1).astype(np.float64)
    large = max_exact + (np.log(d / max_exact) / math.log(MAX_DISTANCE / max_exact) * (N_BUCKETS - max_exact)).astype(np.int32)
    large = np.minimum(large, N_BUCKETS - 1)
    return np.where(dist < max_exact, dist, large).astype(np.int32)


def _bucket_tables():
    return jnp.asarray(np.stack([_bucket_table(r) for _, r in DILATED_PATTERNS]))


def _bias_table_call(rel_bias, buckets):
    def body(rb_ref, bk_ref, o_ref):
        for h in range(N_HEADS):
            bk = bk_ref[h // HEADS_PER_GROUP]

            def step(b, acc):
                return jnp.where(bk == b, rb_ref[b, h], acc)

            o_ref[h] = lax.fori_loop(0, N_BUCKETS, step, jnp.zeros((SPAN, 2 * SPAN), F32))

    return pl.pallas_call(
        body, name="rel_bias_table",
        in_specs=[pl.BlockSpec(memory_space=pltpu.SMEM), pl.BlockSpec(memory_space=pltpu.VMEM)],
        out_specs=pl.BlockSpec(memory_space=pltpu.VMEM),
        out_shape=jax.ShapeDtypeStruct((N_HEADS, SPAN, 2 * SPAN), F32),
    )(rel_bias, buckets)


def _bias_grad_call(dbias, buckets):
    def body(db_ref, bk_ref, o_ref, rows_ref):
        for h in range(N_HEADS):
            bk = bk_ref[h // HEADS_PER_GROUP]
            dv = db_ref[h]

            def step(b, carry):
                rows_ref[h, b] = jnp.sum(jnp.where(bk == b, dv, 0.0), axis=0, keepdims=True)
                return carry

            lax.fori_loop(0, N_BUCKETS, step, 0)
        o_ref[...] = jnp.sum(rows_ref[...], axis=-1, keepdims=True)

    out = pl.pallas_call(
        body, name="rel_bias_grad",
        in_specs=[pl.BlockSpec(memory_space=pltpu.VMEM), pl.BlockSpec(memory_space=pltpu.VMEM)],
        out_specs=pl.BlockSpec(memory_space=pltpu.VMEM),
        out_shape=jax.ShapeDtypeStruct((N_HEADS, N_BUCKETS, 1, 1), F32),
        scratch_shapes=[pltpu.VMEM((N_HEADS, N_BUCKETS, 1, 2 * SPAN), F32)],
    )(dbias, buckets)
    return out.reshape(N_HEADS, N_BUCKETS).T


def _dot_nt(a, b):
    return lax.dot_general(a, b, (((1,), (1,)), ((), ())), preferred_element_type=F32)


def _dot_nn(a, b):
    return lax.dot_general(a, b, (((1,), (0,)), ((), ())), preferred_element_type=F32)


def _dot_tn(a, b):
    return lax.dot_general(a, b, (((0,), (0,)), ((), ())), preferred_element_type=F32)


def _band_masks(n, nb):
    qi = lax.broadcasted_iota(jnp.int32, (SPAN, SPAN), 0)
    ki = lax.broadcasted_iota(jnp.int32, (SPAN, SPAN), 1)
    prev_ok = jnp.logical_and(ki >= qi, n > 0)
    cur_ok = ki <= qi
    next_ok = jnp.logical_and(ki >= qi, n < nb - 1)
    return prev_ok, cur_ok, next_ok


def _wide_band_mask(n):
    qi = lax.broadcasted_iota(jnp.int32, (SPAN, 2 * SPAN), 0)
    ki = lax.broadcasted_iota(jnp.int32, (SPAN, 2 * SPAN), 1)
    prev_ok = jnp.logical_and(jnp.logical_and(ki < SPAN, ki >= qi), n > 0)
    cur_ok = jnp.logical_and(ki >= SPAN, ki - SPAN <= qi)
    return jnp.logical_or(prev_ok, cur_ok)


def _attn_plan(S, group):
    r = DILATED_PATTERNS[group][1]
    hp, per = (HEADS_PER_GROUP, 1) if r == 1 else (2, 4)
    return r, S // (r * SPAN), hp, per


def _residue_rows(rho, r):
    return slice(None) if r == 1 else pl.ds(rho, SPAN, stride=r)


def _for_residues(r, per, fn):
    if r == per:
        for u in range(per):
            fn(u)
        return

    def step(i, carry):
        for u in range(per):
            fn(i * per + u)
        return carry

    lax.fori_loop(0, r // per, step, 0)


def _attn_fwd_call(proj, bias, group):
    S = proj.shape[0]
    r, nb, hp, per = _attn_plan(S, group)
    scale = HEAD_DIM ** -0.5
    kinds = ("q", "kp", "kc", "vp", "vc") if nb > 1 else ("q", "kc", "vc")

    def body(*refs):
        ins = {kind: refs[i * hp:(i + 1) * hp] for i, kind in enumerate(kinds)}
        b_ref, o_ref, lse_ref = refs[len(kinds) * hp:]
        n = pl.program_id(1)
        prev_ok, cur_ok, _ = _band_masks(n, nb)

        band_ok = _wide_band_mask(n) if nb > 1 else cur_ok

        def residue(rho):
            rows = _residue_rows(rho, r)
            for j in range(hp):
                get = lambda kind: ins[kind][j][rows, :].astype(BF16)
                q = get("q")
                if nb > 1:
                    keys, vals, bias_j = jnp.concatenate([get("kp"), get("kc")], axis=0), jnp.concatenate([get("vp"), get("vc")], axis=0), b_ref[j]
                else:
                    keys, vals, bias_j = get("kc"), get("vc"), b_ref[j, :, SPAN:]
                s = jnp.where(band_ok, _dot_nt(q, keys) * scale + bias_j, NEG_INF)
                m = jnp.max(s, axis=-1, keepdims=True)
                p = jnp.exp(s - m)
                den = jnp.sum(p, axis=-1, keepdims=True)
                o_ref[j, rows, :] = _dot_nn(p.astype(BF16), vals) / den
                lse_ref[j, rows, :] = jnp.broadcast_to(m + jnp.log(den), (SPAN, HEAD_DIM))

        _for_residues(r, per, residue)

    in_specs = [_head_spec(r, nb, hp, kind, group, jj) for kind in kinds for jj in range(hp)]
    in_specs.append(pl.BlockSpec((hp, SPAN, 2 * SPAN), lambda j, n: (group * (HEADS_PER_GROUP // hp) + j, 0, 0)))
    out = pl.BlockSpec((hp, r * SPAN, HEAD_DIM), lambda j, n: (j, n, 0))
    return pl.pallas_call(
        body, name=f"attn_fwd_g{group}", grid=(HEADS_PER_GROUP // hp, nb),
        in_specs=in_specs, out_specs=[out] * 2,
        out_shape=[jax.ShapeDtypeStruct((HEADS_PER_GROUP, S, HEAD_DIM), F32)] * 2,
        compiler_params=_params(("parallel", "parallel"), VMEM_LIMIT),
    )(*([proj] * (len(in_specs) - 1)), bias)


_PROJ_PART = dict(q=0, qn=0, kp=1, kc=1, vp=2, vc=2)


def _head_spec(r, nb, hp, kind, group, jj):
    if kind in _PROJ_PART:
        base = (_PROJ_PART[kind] * N_GROUPS + group) * HEADS_PER_GROUP
    else:
        base = 0
    if kind.endswith("p"):
        row = lambda n: jnp.maximum(n - 1, 0)
    elif kind.endswith("n"):
        row = lambda n: jnp.minimum(n + 1, nb - 1)
    else:
        row = lambda n: n
    return pl.BlockSpec((r * SPAN, HEAD_DIM), lambda j, n: (row(n), base + j * hp + jj))


def _attn_merge_call(parts):
    S = parts[0].shape[1]

    def body(o1, s1, o2, s2, o3, s3, a_ref, ab_ref, lse_ref):
        for j in range(HEADS_PER_GROUP):
            sl = slice(j * HEAD_DIM, (j + 1) * HEAD_DIM)
            mx = jnp.maximum(jnp.maximum(s1[j], s2[j]), s3[j])
            w1 = jnp.exp(s1[j] - mx)
            w2 = jnp.exp(s2[j] - mx)
            w3 = jnp.exp(s3[j] - mx)
            den = w1 + w2 + w3
            a = (w1 * o1[j] + w2 * o2[j] + w3 * o3[j]) / den
            a_ref[:, sl] = a
            ab_ref[:, sl] = a.astype(BF16)
            lse_ref[:, sl] = mx + jnp.log(den)

    heads = pl.BlockSpec((HEADS_PER_GROUP, ROW_TILE, HEAD_DIM), lambda i: (0, i, 0))
    return pl.pallas_call(
        body, name="attn_merge", grid=(S // ROW_TILE,),
        in_specs=[heads] * 6, out_specs=[_row_spec(GROUP_WIDTH)] * 3,
        out_shape=[jax.ShapeDtypeStruct((S, GROUP_WIDTH), F32), jax.ShapeDtypeStruct((S, GROUP_WIDTH), BF16),
                   jax.ShapeDtypeStruct((S, GROUP_WIDTH), F32)],
        compiler_params=_params(("parallel",)),
    )(*parts)


def _attn_delta_call(a, da):
    S = a.shape[0]

    def body(a_ref, da_ref, d_ref):
        for j in range(HEADS_PER_GROUP):
            sl = slice(j * HEAD_DIM, (j + 1) * HEAD_DIM)
            d = jnp.sum(a_ref[:, sl] * da_ref[:, sl], axis=-1, keepdims=True)
            d_ref[:, sl] = jnp.broadcast_to(d, (ROW_TILE, HEAD_DIM))

    return pl.pallas_call(
        body, name="attn_delta", grid=(S // ROW_TILE,),
        in_specs=[_row_spec(GROUP_WIDTH)] * 2, out_specs=_row_spec(GROUP_WIDTH),
        out_shape=jax.ShapeDtypeStruct((S, GROUP_WIDTH), F32),
        compiler_params=_params(("parallel",)),
    )(a, da)


def _attn_bwd_call(proj, bias, da, lse, delta, group):
    S = proj.shape[0]
    r, nb, hp, per = _attn_plan(S, group)
    scale = HEAD_DIM ** -0.5
    kinds = ("q", "qn", "kp", "kc", "vp", "vc", "da", "dan", "lse", "lsen", "dl", "dln") if nb > 1 else ("q", "kc", "vc", "da", "lse", "dl")
    source = dict(da=da, dan=da, lse=lse, lsen=lse, dl=delta, dln=delta)

    def body(*refs):
        ins = {kind: refs[i * hp:(i + 1) * hp] for i, kind in enumerate(kinds)}
        b_ref, dq_ref, dk_ref, dv_ref, db_ref = refs[len(kinds) * hp:]
        n = pl.program_id(1)
        prev_ok, cur_ok, next_ok = _band_masks(n, nb)

        @pl.when(n == 0)
        def _():
            db_ref[...] = jnp.zeros_like(db_ref)

        band_ok = _wide_band_mask(n) if nb > 1 else cur_ok

        def residue(rho):
            rows = _residue_rows(rho, r)
            for j in range(hp):
                get = lambda kind: ins[kind][j][rows, :]
                q = get("q").astype(BF16)
                kc = get("kc").astype(BF16)
                vc = get("vc").astype(BF16)
                dav = get("da").astype(BF16)
                lse_q, dl_q = get("lse"), get("dl")
                if nb == 1:
                    pc = jnp.exp(jnp.where(cur_ok, _dot_nt(q, kc) * scale + b_ref[j, :, SPAN:], NEG_INF) - lse_q)
                    dsc = pc * (_dot_nt(dav, vc) - dl_q)
                    dsc_b = dsc.astype(BF16)
                    dq = _dot_nn(dsc_b, kc)
                    dk = _dot_tn(dsc_b, q)
                    dv = _dot_tn(pc.astype(BF16), dav)
                    db_ref[j, :, SPAN:] += dsc
                else:
                    qn = get("qn").astype(BF16)
                    dan = get("dan").astype(BF16)
                    keys = jnp.concatenate([get("kp").astype(BF16), kc], axis=0)
                    vals = jnp.concatenate([get("vp").astype(BF16), vc], axis=0)
                    wide = lambda t: jnp.concatenate([t, t], axis=1)
                    p = jnp.exp(jnp.where(band_ok, _dot_nt(q, keys) * scale + b_ref[j], NEG_INF) - wide(lse_q))
                    ds = p * (_dot_nt(dav, vals) - wide(dl_q))
                    dq = _dot_nn(ds.astype(BF16), keys)
                    db_ref[j] += ds
                    pn = jnp.exp(jnp.where(next_ok, _dot_nt(qn, kc) * scale + b_ref[j, :, :SPAN], NEG_INF) - get("lsen"))
                    dsn = pn * (_dot_nt(dan, vc) - get("dln"))
                    both = lambda cur_part, next_part: jnp.concatenate([cur_part.astype(BF16), next_part.astype(BF16)], axis=0)
                    dk = _dot_tn(both(ds[:, SPAN:], dsn), jnp.concatenate([q, qn], axis=0))
                    dv = _dot_tn(both(p[:, SPAN:], pn), jnp.concatenate([dav, dan], axis=0))
                dq_ref[j, rows, :] = dq * scale
                dk_ref[j, rows, :] = dk * scale
                dv_ref[j, rows, :] = dv

        _for_residues(r, per, residue)

    per_group = HEADS_PER_GROUP // hp
    band = (hp, SPAN, 2 * SPAN)
    in_specs = [_head_spec(r, nb, hp, kind, group, jj) for kind in kinds for jj in range(hp)]
    in_specs.append(pl.BlockSpec(band, lambda j, n: (group * per_group + j, 0, 0)))
    operands = [source.get(kind, proj) for kind in kinds for _ in range(hp)] + [bias]
    out = pl.BlockSpec((hp, r * SPAN, HEAD_DIM), lambda j, n: (j, n, 0))
    return pl.pallas_call(
        body, name=f"attn_bwd_g{group}", grid=(per_group, nb),
        in_specs=in_specs,
        out_specs=[out] * 3 + [pl.BlockSpec(band, lambda j, n: (j, 0, 0))],
        out_shape=[jax.ShapeDtypeStruct((HEADS_PER_GROUP, S, HEAD_DIM), F32)] * 3
        + [jax.ShapeDtypeStruct((HEADS_PER_GROUP, SPAN, 2 * SPAN), F32)],
        compiler_params=_params(("parallel", "arbitrary"), VMEM_LIMIT),
    )(*operands)


def _dproj_call(dqkv, tails):
    S = tails[0].shape[0]
    width = len(dqkv) * GROUP_WIDTH + sum(t.shape[1] for t in tails)

    def body(*refs):
        o_ref = refs[-1]
        col = 0
        for ref in refs[:len(dqkv)]:
            for j in range(HEADS_PER_GROUP):
                o_ref[:, col:col + HEAD_DIM] = ref[j].astype(BF16)
                col += HEAD_DIM
        for ref in refs[len(dqkv):-1]:
            o_ref[:, col:col + ref.shape[1]] = ref[...]
            col += ref.shape[1]

    heads = pl.BlockSpec((HEADS_PER_GROUP, ROW_TILE, HEAD_DIM), lambda i: (0, i, 0))
    return pl.pallas_call(
        body, name="dproj_assemble", grid=(S // ROW_TILE,),
        in_specs=[heads] * len(dqkv) + [_row_spec(t.shape[1]) for t in tails],
        out_specs=_row_spec(width), out_shape=jax.ShapeDtypeStruct((S, width), BF16),
        compiler_params=_params(("parallel",)),
    )(*dqkv, *tails)


def _tap_rows(xpad_ref, t0, k, width, pad):
    return xpad_ref[pl.ds(t0 + (pad - (width - 1 - k)), TIME_BLOCK), :]


def _conv_block(xpad_ref, t0, w_ref, width, pad):
    acc = None
    for k in range(width):
        term = w_ref[k:k + 1, :] * _tap_rows(xpad_ref, t0, k, width, pad)
        acc = term if acc is None else acc + term
    return acc


def _conv_transpose_block(dpad_ref, t0, w_ref, width):
    acc = None
    for k in range(width):
        term = w_ref[k:k + 1, :] * dpad_ref[pl.ds(t0 + (width - 1 - k), TIME_BLOCK), :]
        acc = term if acc is None else acc + term
    return acc


def _conv_weight_grad(xpad_ref, t0, dy, dw_ref, width, pad):
    for k in range(width):
        dw_ref[k:k + 1, :] += jnp.sum(dy * _tap_rows(xpad_ref, t0, k, width, pad), axis=0, keepdims=True)


def _time_loop(S, step):
    def it(tb, carry):
        step(pl.multiple_of(tb * TIME_BLOCK, TIME_BLOCK))
        return carry

    lax.fori_loop(0, S // TIME_BLOCK, it, 0)


def _conv_fwd_call(proj, col0, w, b):
    S = proj.shape[0]
    C = w.shape[1]
    nt = C // LANES
    v0, g0 = col0 // LANES, (col0 + C) // LANES

    def body(val_ref, gate_ref, w_ref, b_ref, o_ref, pad_ref):
        pad_ref[0:CONV_PAD, :] = jnp.zeros((CONV_PAD, LANES), F32)
        pad_ref[CONV_PAD:, :] = val_ref[...] * _sigmoid(gate_ref[...])

        def step(t0):
            o_ref[pl.ds(t0, TIME_BLOCK), :] = _conv_block(pad_ref, t0, w_ref, CONV_WIDTH, CONV_PAD) + b_ref[...]

        _time_loop(S, step)

    seq = lambda off: pl.BlockSpec((S, LANES), lambda i: (0, off + i))
    return pl.pallas_call(
        body, name="conv_module", grid=(nt,),
        in_specs=[seq(v0), seq(g0), pl.BlockSpec((CONV_WIDTH, LANES), lambda i: (0, i)), pl.BlockSpec((1, LANES), lambda i: (0, i))],
        out_specs=seq(0), out_shape=jax.ShapeDtypeStruct((S, C), F32),
        scratch_shapes=[pltpu.VMEM((S + CONV_PAD, LANES), F32)],
        compiler_params=_params(("parallel",)),
    )(proj, proj, w, b)


def _conv_bwd_call(proj, col0, w, dc1):
    S = proj.shape[0]
    C = w.shape[1]
    nt = C // LANES
    v0, g0 = col0 // LANES, (col0 + C) // LANES

    def body(val_ref, gate_ref, w_ref, dy_ref, dval_ref, dgate_ref, dw_ref, db_ref, xpad_ref, dpad_ref, dwacc_ref):
        xpad_ref[0:CONV_PAD, :] = jnp.zeros((CONV_PAD, LANES), F32)
        xpad_ref[CONV_PAD:, :] = val_ref[...] * _sigmoid(gate_ref[...])
        dpad_ref[0:S, :] = dy_ref[...]
        dpad_ref[S:, :] = jnp.zeros((CONV_PAD, LANES), F32)
        dwacc_ref[...] = jnp.zeros_like(dwacc_ref)

        def step(t0):
            rows = pl.ds(t0, TIME_BLOCK)
            _conv_weight_grad(xpad_ref, t0, dy_ref[rows, :], dwacc_ref, CONV_WIDTH, CONV_PAD)
            dc0 = _conv_transpose_block(dpad_ref, t0, w_ref, CONV_WIDTH)
            sg = _sigmoid(gate_ref[rows, :])
            dval_ref[rows, :] = (dc0 * sg).astype(BF16)
            dgate_ref[rows, :] = (dc0 * val_ref[rows, :] * sg * (1.0 - sg)).astype(BF16)

        _time_loop(S, step)
        dw_ref[...] = dwacc_ref[...]
        db_ref[...] = jnp.sum(dy_ref[...], axis=0, keepdims=True)

    seq = lambda off: pl.BlockSpec((S, LANES), lambda i: (0, off + i))
    return pl.pallas_call(
        body, name="conv_module_bwd", grid=(nt,),
        in_specs=[seq(v0), seq(g0), pl.BlockSpec((CONV_WIDTH, LANES), lambda i: (0, i)), seq(0)],
        out_specs=[seq(0), seq(0), pl.BlockSpec((CONV_PAD, LANES), lambda i: (0, i)), pl.BlockSpec((1, LANES), lambda i: (0, i))],
        out_shape=[jax.ShapeDtypeStruct((S, C), BF16), jax.ShapeDtypeStruct((S, C), BF16),
                   jax.ShapeDtypeStruct((CONV_PAD, C), F32), jax.ShapeDtypeStruct((1, C), F32)],
        scratch_shapes=[pltpu.VMEM((S + CONV_PAD, LANES), F32), pltpu.VMEM((S + CONV_PAD, LANES), F32),
                        pltpu.VMEM((CONV_PAD, LANES), F32)],
        compiler_params=_params(("parallel",)),
    )(proj, proj, w, dc1)


def _ffn_fwd_call(u, w, b):
    S, C2 = u.shape
    C = C2 // 2
    nt = C // LANES

    def body(ug_ref, uv_ref, wg_ref, wv_ref, bg_ref, bv_ref, f_ref, pg_ref, pv_ref):
        zeros = jnp.zeros((FFN_PAD, LANES), F32)
        pg_ref[0:FFN_PAD, :] = zeros
        pv_ref[0:FFN_PAD, :] = zeros
        pg_ref[FFN_PAD:, :] = ug_ref[...]
        pv_ref[FFN_PAD:, :] = uv_ref[...]

        def step(t0):
            cg = _conv_block(pg_ref, t0, wg_ref, FFN_CONV_WIDTH, FFN_PAD) + bg_ref[...]
            cv = _conv_block(pv_ref, t0, wv_ref, FFN_CONV_WIDTH, FFN_PAD) + bv_ref[...]
            f_ref[pl.ds(t0, TIME_BLOCK), :] = (_gelu(cg) * cv).astype(BF16)

        _time_loop(S, step)

    seq = lambda off: pl.BlockSpec((S, LANES), lambda i: (0, off + i))
    wsp = lambda off: pl.BlockSpec((FFN_CONV_WIDTH, LANES), lambda i: (0, off + i))
    bsp = lambda off: pl.BlockSpec((1, LANES), lambda i: (0, off + i))
    return pl.pallas_call(
        body, name="ffn_conv_geglu", grid=(nt,),
        in_specs=[seq(0), seq(nt), wsp(0), wsp(nt), bsp(0), bsp(nt)],
        out_specs=seq(0), out_shape=jax.ShapeDtypeStruct((S, C), BF16),
        scratch_shapes=[pltpu.VMEM((S + FFN_PAD, LANES), F32)] * 2,
        compiler_params=_params(("parallel",)),
    )(u, u, w, w, b, b)


def _ffn_bwd_call(u, w, b, df):
    S, C2 = u.shape
    C = C2 // 2
    nt = C // LANES

    def body(ug_ref, uv_ref, wg_ref, wv_ref, bg_ref, bv_ref, df_ref,
             du_ref, dwg_ref, dwv_ref, dbg_ref, dbv_ref,
             pg_ref, pv_ref, dg_ref, dv_ref, dwg_acc, dwv_acc, dbg_acc, dbv_acc):
        zeros = jnp.zeros((FFN_PAD, LANES), F32)
        pg_ref[0:FFN_PAD, :] = zeros
        pv_ref[0:FFN_PAD, :] = zeros
        pg_ref[FFN_PAD:, :] = ug_ref[...]
        pv_ref[FFN_PAD:, :] = uv_ref[...]
        dg_ref[S:, :] = zeros
        dv_ref[S:, :] = zeros
        dwg_acc[...] = jnp.zeros_like(dwg_acc)
        dwv_acc[...] = jnp.zeros_like(dwv_acc)
        dbg_acc[...] = jnp.zeros_like(dbg_acc)
        dbv_acc[...] = jnp.zeros_like(dbv_acc)

        def first(t0):
            rows = pl.ds(t0, TIME_BLOCK)
            cg = _conv_block(pg_ref, t0, wg_ref, FFN_CONV_WIDTH, FFN_PAD) + bg_ref[...]
            cv = _conv_block(pv_ref, t0, wv_ref, FFN_CONV_WIDTH, FFN_PAD) + bv_ref[...]
            dfb = df_ref[rows, :]
            gelu, gelu_grad = _gelu_and_grad(cg)
            dcg = dfb * cv * gelu_grad
            dcv = dfb * gelu
            dg_ref[rows, :] = dcg
            dv_ref[rows, :] = dcv
            _conv_weight_grad(pg_ref, t0, dcg, dwg_acc, FFN_CONV_WIDTH, FFN_PAD)
            _conv_weight_grad(pv_ref, t0, dcv, dwv_acc, FFN_CONV_WIDTH, FFN_PAD)
            dbg_acc[...] += jnp.sum(dcg, axis=0, keepdims=True)
            dbv_acc[...] += jnp.sum(dcv, axis=0, keepdims=True)

        def second(t0):
            rows = pl.ds(t0, TIME_BLOCK)
            du_ref[0, rows, :] = _conv_transpose_block(dg_ref, t0, wg_ref, FFN_CONV_WIDTH).astype(BF16)
            du_ref[1, rows, :] = _conv_transpose_block(dv_ref, t0, wv_ref, FFN_CONV_WIDTH).astype(BF16)

        _time_loop(S, first)
        _time_loop(S, second)
        dwg_ref[...] = dwg_acc[...]
        dwv_ref[...] = dwv_acc[...]
        dbg_ref[...] = dbg_acc[...]
        dbv_ref[...] = dbv_acc[...]

    seq = lambda off: pl.BlockSpec((S, LANES), lambda i: (0, off + i))
    wsp = lambda off: pl.BlockSpec((FFN_CONV_WIDTH, LANES), lambda i: (0, off + i))
    bsp = lambda off: pl.BlockSpec((1, LANES), lambda i: (0, off + i))
    return pl.pallas_call(
        body, name="ffn_conv_geglu_bwd", grid=(nt,),
        in_specs=[seq(0), seq(nt), wsp(0), wsp(nt), bsp(0), bsp(nt), seq(0)],
        out_specs=[pl.BlockSpec((2, S, LANES), lambda i: (0, 0, i)),
                   pl.BlockSpec((SUBLANES, LANES), lambda i: (0, i)), pl.BlockSpec((SUBLANES, LANES), lambda i: (0, i)),
                   bsp(0), bsp(0)],
        out_shape=[jax.ShapeDtypeStruct((2, S, C), BF16)] + [jax.ShapeDtypeStruct((SUBLANES, C), F32)] * 2
        + [jax.ShapeDtypeStruct((1, C), F32)] * 2,
        scratch_shapes=[pltpu.VMEM((S + FFN_PAD, LANES), F32)] * 4 + [pltpu.VMEM((SUBLANES, LANES), F32)] * 2
        + [pltpu.VMEM((1, LANES), F32)] * 2,
        compiler_params=_params(("parallel",)),
    )(u, u, w, w, b, b, df)


def _adamw(w_ref, g_ref, m_ref, v_ref, d_ref, mo_ref, vo_ref):
    gv = g_ref[...]
    mn = ADAM_B1 * m_ref[...] + (1.0 - ADAM_B1) * gv
    vn = ADAM_B2 * v_ref[...] + (1.0 - ADAM_B2) * (gv * gv)
    mo_ref[...] = mn
    vo_ref[...] = vn
    m_hat = mn * (1.0 / (1.0 - ADAM_B1 ** ADAM_STEP))
    v_hat = vn * (1.0 / (1.0 - ADAM_B2 ** ADAM_STEP))
    d_ref[...] = -ADAM_LR * (m_hat / (jnp.sqrt(v_hat) + ADAM_EPS) + ADAM_WD * w_ref[...])


def _adamw_call(w, g, m, v, name):
    R, C = w.shape
    tr = _row_tile(R, C)
    spec = pl.BlockSpec((tr, C), lambda i: (i, 0))
    return pl.pallas_call(
        _adamw_body(), name=name, grid=(R // tr,),
        in_specs=[spec] * 4, out_specs=[spec] * 3,
        out_shape=[jax.ShapeDtypeStruct((R, C), F32)] * 3,
        compiler_params=_params(("parallel",)),
    )(w, g, m, v)


def _adamw_body():
    def body(*refs):
        _adamw(*refs)

    return body


def _adamw_small_call(ws, gs, ms, vs):
    n = len(ws)

    def body(*refs):
        w_refs, g_refs, m_refs, v_refs, d_refs, mo_refs, vo_refs = (refs[i * n:(i + 1) * n] for i in range(7))
        for i in range(n):
            _adamw(w_refs[i], g_refs[i], m_refs[i], v_refs[i], d_refs[i], mo_refs[i], vo_refs[i])

    whole = pl.BlockSpec(memory_space=pltpu.VMEM)
    outs = pl.pallas_call(
        body, name="adamw_small",
        in_specs=[whole] * (4 * n), out_specs=[whole] * (3 * n),
        out_shape=[jax.ShapeDtypeStruct(w.shape, F32) for w in ws] * 3,
    )(*ws, *gs, *ms, *vs)
    return outs[:n], outs[n:2 * n], outs[2 * n:]


def _position():
    return lax.axis_index("x"), lax.axis_index("y"), lax.axis_index("c")


def _chip_peers(x, y):
    return [(x, 1 - y), (1 - x, y), (1 - x, 1 - y)]


def _half_rows(ref, core, rows):
    h = rows // 2
    start = pl.multiple_of(core * h, 16)
    return ref.at[pl.ds(start, h), :] if len(ref.shape) == 2 else ref.at[:, pl.ds(start, h), :]


def _shard_half(ref, shard, core, rows):
    h = rows // 2
    return ref.at[shard, pl.ds(pl.multiple_of(core * h, 16), h), :]


ANY = pl.BlockSpec(memory_space=pl.ANY)


def _allgather_call(shards, whole):
    n, nw = len(shards), len(whole)
    outs_shape = [jax.ShapeDtypeStruct((N_CHIPS,) + s.shape, s.dtype) for s in shards + whole]

    def body(*refs):
        ins, outs = refs[:n + nw], refs[n + nw:2 * (n + nw)]
        send_sems, recv_sems, pass_send, pass_recv, own_send, own_recv = refs[2 * (n + nw):]
        x, y, c = _position()
        chip = 2 * x + y
        peers = _chip_peers(x, y)
        sent, local = [], []
        for i in range(n + nw):
            cp = pltpu.make_async_remote_copy(src_ref=ins[i], dst_ref=outs[i].at[chip], send_sem=own_send.at[i],
                                              recv_sem=own_recv.at[i], device_id=(x, y, 1 - c), device_id_type=MESH)
            cp.start()
            local.append(cp)
            rows = ins[i].shape[0]
            for k, (px, py) in enumerate(peers):
                if i < n:
                    src, dst = _half_rows(ins[i], c, rows), _shard_half(outs[i], chip, c, rows)
                else:
                    src, dst = ins[i], outs[i].at[chip]
                cp = pltpu.make_async_remote_copy(src_ref=src, dst_ref=dst, send_sem=send_sems.at[i, k],
                                                  recv_sem=recv_sems.at[i, k], device_id=(px, py, c), device_id_type=MESH)
                cp.start()
                sent.append(cp)
        passed = []
        for i in range(n + nw):
            rows = ins[i].shape[0]
            for k, (px, py) in enumerate(peers):
                landed = _shard_half(outs[i], 2 * px + py, c, rows) if i < n else outs[i].at[2 * px + py]
                pltpu.make_async_remote_copy(src_ref=landed, dst_ref=landed, send_sem=send_sems.at[i, k],
                                             recv_sem=recv_sems.at[i, k], device_id=(px, py, c), device_id_type=MESH).wait_recv()
                if i < n:
                    cp = pltpu.make_async_remote_copy(src_ref=landed, dst_ref=landed, send_sem=pass_send.at[i, k],
                                                      recv_sem=pass_recv.at[i, k], device_id=(x, y, 1 - c), device_id_type=MESH)
                    cp.start()
                    passed.append(cp)
        for cp in sent:
            cp.wait_send()
        for cp in passed:
            cp.wait()
        for cp in local:
            cp.wait()

    return pl.pallas_call(
        body, name="weight_allgather",
        in_specs=[ANY] * (n + nw), out_specs=[ANY] * (n + nw), out_shape=outs_shape,
        scratch_shapes=[pltpu.SemaphoreType.DMA((n + nw, 3)), pltpu.SemaphoreType.DMA((n + nw, 3)),
                        pltpu.SemaphoreType.DMA((n, 3)), pltpu.SemaphoreType.DMA((n, 3)),
                        pltpu.SemaphoreType.DMA((n + nw,)), pltpu.SemaphoreType.DMA((n + nw,))],
    )(*shards, *whole)


HBM_SPEC = pl.BlockSpec(memory_space=pltpu.HBM)
SEM_SPEC = pl.BlockSpec(memory_space=pltpu.SEMAPHORE)
DATAFLOW = pltpu.SideEffectType.DATAFLOW_SIDE_EFFECTING


def _in_hbm(a):
    return pltpu.with_memory_space_constraint(a, pltpu.HBM)


def _split_start(name, groups, after):
    spans, arrays = [], []
    for srcs, lands, _, _ in groups:
        spans.append((len(arrays), len(srcs), len(lands)))
        arrays += list(srcs) + list(lands)
    na, ng = len(arrays), len(groups)

    def body(*refs):
        sems, token = refs[na + 1:na + 1 + 2 * ng], refs[-1]
        for g, (_, _, _, copies) in enumerate(groups):
            off, ns, nl = spans[g]
            for src, dst, dev, idx in copies(refs[off:off + ns], refs[off + ns:off + ns + nl]):
                pltpu.make_async_remote_copy(src_ref=src, dst_ref=dst, send_sem=sems[2 * g].at[idx], recv_sem=sems[2 * g + 1].at[idx],
                                             device_id=dev, device_id_type=MESH).start()
        token[...] = jnp.zeros_like(token)

    outs = pl.pallas_call(
        body, name=name,
        in_specs=[HBM_SPEC] * na + [ANY],
        out_specs=[SEM_SPEC] * (2 * ng) + [HBM_SPEC] * na + [pl.BlockSpec(memory_space=pltpu.VMEM)],
        out_shape=[pltpu.SemaphoreType.DMA((n_sems,)) for _, _, n_sems, _ in groups for _ in range(2)]
        + [pltpu.HBM(a.shape, a.dtype) for a in arrays] + [jax.ShapeDtypeStruct((SUBLANES, LANES), F32)],
        input_output_aliases={i: 2 * ng + i for i in range(na)},
        compiler_params=pltpu.CompilerParams(has_side_effects=DATAFLOW),
    )(*[_in_hbm(a) for a in arrays], after)
    started = []
    for g, (off, ns, nl) in enumerate(spans):
        thru = outs[2 * ng + off:2 * ng + off + ns + nl]
        started.append(dict(send=outs[2 * g], recv=outs[2 * g + 1], srcs=list(thru[:ns]), lands=list(thru[ns:]),
                            tile=outs[-1], token=outs[-1][0, 0]))
    return started


def _split_wait(name, started, copies, after):
    n, m = len(started["srcs"]), len(started["lands"])

    def body(*refs):
        src_refs, land_refs = refs[:n], refs[n:n + m]
        send_sem, recv_sem = refs[n + m], refs[n + m + 1]
        for src, dst, dev, idx in copies(src_refs, land_refs):
            cp = pltpu.make_async_remote_copy(src_ref=src, dst_ref=dst, send_sem=send_sem.at[idx], recv_sem=recv_sem.at[idx],
                                              device_id=dev, device_id_type=MESH)
            cp.wait_send()
            cp.wait_recv()

    arrays = started["srcs"] + started["lands"]
    outs = pl.pallas_call(
        body, name=name,
        in_specs=[HBM_SPEC] * (n + m) + [SEM_SPEC, SEM_SPEC, ANY],
        out_specs=[HBM_SPEC] * (n + m),
        out_shape=[pltpu.HBM(a.shape, a.dtype) for a in arrays],
        input_output_aliases={i: i for i in range(n + m)},
        compiler_params=pltpu.CompilerParams(has_side_effects=DATAFLOW),
    )(*arrays, started["send"], started["recv"], after)
    return list(outs)


def _gather_copies(srcs, lands):
    x, y, c = _position()
    chip = 2 * x + y
    targets = [(px, py, c) for px, py in _chip_peers(x, y)] + [(x, y, 1 - c)]
    return [(s, l.at[chip], dev, len(targets) * i + k) for i, (s, l) in enumerate(zip(srcs, lands)) for k, dev in enumerate(targets)]


def _sibling_copies(srcs, lands):
    x, y, c = _position()
    return [(_half_rows(srcs[0], 1 - c, srcs[0].shape[1]), lands[0], (x, y, 1 - c), 0)]


def _exchange_copies(srcs, lands):
    x, y, c = _position()
    return [(srcs[0].at[2 * px + py], lands[0].at[k], (px, py, c), k) for k, (px, py) in enumerate(_chip_peers(x, y))]


def _pair_sum_call(grad, recv, core, name):
    _, h, B = recv.shape
    tr = _row_tile(h, B)

    def body(core_ref, g_ref, r_ref, o_ref, ob_ref):
        s = g_ref[...] + r_ref[...]
        o_ref[...] = s
        ob_ref[...] = s.astype(BF16)

    g_spec = pl.BlockSpec((None, tr, B), lambda q, i, core_ref: (q, core_ref[0] * (h // tr) + i, 0))
    spec = pl.BlockSpec((None, tr, B), lambda q, i, core_ref: (q, i, 0))
    return pl.pallas_call(
        body, name=name,
        grid_spec=pltpu.PrefetchScalarGridSpec(num_scalar_prefetch=1, grid=(N_CHIPS, h // tr), in_specs=[g_spec, spec],
                                               out_specs=[spec, spec]),
        out_shape=[jax.ShapeDtypeStruct(recv.shape, F32), jax.ShapeDtypeStruct(recv.shape, BF16)],
        compiler_params=_params(("parallel", "parallel")),
    )(core, grad, recv)


def _chip_sum_call(partial, recv, chip_core, name):
    _, h, B = recv.shape
    tr = _row_tile(h, B)

    def body(cc_ref, p_ref, r_ref, o_ref):
        o_ref[...] = ((p_ref[...] + r_ref[0].astype(F32)) + r_ref[1].astype(F32)) + r_ref[2].astype(F32)

    return pl.pallas_call(
        body, name=name,
        grid_spec=pltpu.PrefetchScalarGridSpec(
            num_scalar_prefetch=1, grid=(h // tr,),
            in_specs=[pl.BlockSpec((None, tr, B), lambda i, cc_ref: (cc_ref[0], i, 0)),
                      pl.BlockSpec((3, tr, B), lambda i, cc_ref: (0, i, 0))],
            out_specs=pl.BlockSpec((tr, B), lambda i, cc_ref: (cc_ref[1] * (h // tr) + i, 0))),
        out_shape=jax.ShapeDtypeStruct((2 * h, B), F32),
        compiler_params=_params(("parallel",)),
    )(chip_core, partial, recv)


def _sibling_assemble_call(shards, name="grad_sibling_assemble"):
    n = len(shards)

    def body(*refs):
        ins, outs = refs[:n], refs[n:2 * n]
        send_sems, recv_sems = refs[2 * n:]
        x, y, c = _position()
        copies = []
        for i in range(n):
            rows = shards[i].shape[0]
            cp = pltpu.make_async_remote_copy(src_ref=_half_rows(ins[i], c, rows), dst_ref=_half_rows(outs[i], c, rows),
                                              send_sem=send_sems.at[i], recv_sem=recv_sems.at[i],
                                              device_id=(x, y, 1 - c), device_id_type=MESH)
            cp.start()
            copies.append(cp)
        for cp in copies:
            cp.wait()

    return pl.pallas_call(
        body, name=name,
        in_specs=[ANY] * n, out_specs=[ANY] * n,
        out_shape=[jax.ShapeDtypeStruct(s.shape, F32) for s in shards],
        input_output_aliases={i: i for i in range(n)},
        scratch_shapes=[pltpu.SemaphoreType.DMA((n,)), pltpu.SemaphoreType.DMA((n,))],
    )(*shards)


N_DEVICES = 8


def _allsum_copies(srcs, lands):
    x, y, c = _position()
    me = 4 * x + 2 * y + c
    out = []
    for k in range(1, N_DEVICES):
        peer = (1 - x if k & 4 else x, 1 - y if k & 2 else y, 1 - c if k & 1 else c)
        out.append((srcs[0], lands[0].at[me], peer, k - 1))
    return out


def _ordered_sum_call(mine, landed, me):
    rows = mine.shape[0]

    def body(me_ref, x_ref, l_ref, o_ref):
        acc = jnp.where(me_ref[0] == 0, x_ref[...], l_ref[0])
        for d in range(1, N_DEVICES):
            acc = acc + jnp.where(me_ref[0] == d, x_ref[...], l_ref[d])
        o_ref[...] = acc

    return pl.pallas_call(
        body, name="small_grad_sum",
        in_specs=[pl.BlockSpec(memory_space=pltpu.SMEM), pl.BlockSpec(memory_space=pltpu.VMEM), pl.BlockSpec(memory_space=pltpu.VMEM)],
        out_specs=pl.BlockSpec(memory_space=pltpu.VMEM),
        out_shape=jax.ShapeDtypeStruct((rows, LANES), F32),
    )(me, mine, landed)


def _pack(arrays):
    flat = jnp.concatenate([a.reshape(-1).astype(F32) for a in arrays])
    rows = -(-flat.shape[0] // LANES)
    rows = -(-rows // SUBLANES) * SUBLANES
    flat = jnp.pad(flat, (0, rows * LANES - flat.shape[0]))
    return flat.reshape(rows, LANES)


def _unpack(packed, shapes):
    flat = packed.reshape(-1)
    out, off = [], 0
    for shp in shapes:
        size = int(np.prod(shp))
        out.append(flat[off:off + size].reshape(shp))
        off += size
    return out


def _local_step(xs, target, P, late_weights, on_grad):
    S, D = xs.shape
    qkv_width = 3 * N_HEADS * HEAD_DIM
    glu_col0, gate_col0 = qkv_width, qkv_width + 2 * D
    shard_major = lambda g: g.reshape(N_CHIPS, g.shape[0] // N_CHIPS, g.shape[1])

    h1 = _rms_fwd_call(xs, P["norm_mix_pre"])
    proj = _matmul(h1, P["w_in"], "nn", "proj_in")
    buckets = _bucket_tables()
    bias = _bias_table_call(P["rel_bias"], buckets)
    parts = []
    for g in range(N_GROUPS):
        parts += _attn_fwd_call(proj, bias, g)
    a, a_bf, lse = _attn_merge_call(parts)
    P = dict(P, **late_weights("mix", a_bf))
    y_a = _matmul(a_bf, P["w_attn_out"], "nn", "attn_out")
    c1 = _conv_fwd_call(proj, glu_col0, P["conv_dw_w"], P["conv_dw_b"])
    cact = _ln_silu_call(c1, P["conv_ln_g"], P["conv_ln_b"])
    y_c = _matmul(cact, P["conv_pw_w"], "nn", "conv_pw")
    mixed = _mix_call(proj, gate_col0, P["b_gate"], y_a, y_c)
    out = _matmul(mixed, P["w_out"], "nn", "mix_out")
    x1, h2 = _res1_call(xs, out, P["norm_mix_post"], P["norm_ffn_pre"])
    P = dict(P, **late_weights("ffn", h2))
    u = _matmul(h2, P["w_up"], "nn", "ffn_up")
    f = _ffn_fwd_call(u, P["ffn_conv_w"], P["ffn_conv_b"])
    yff = _matmul(f, P["w_down"], "nn", "ffn_down")
    loss_tile, dx2, dyff, dg_ffn_post = _loss_call(yff, x1, P["norm_ffn_post"], target)

    G = {}
    G["norm_ffn_post"] = dg_ffn_post
    zero = on_grad("w_down", shard_major(_matmul(f, dyff, "tn", "ffn_down_dw")))
    df = _matmul(dyff, P["w_down"], "nt", "ffn_down_dx")
    du, dwg, dwv, dbg, dbv = _ffn_bwd_call(u, P["ffn_conv_w"], P["ffn_conv_b"] + zero, df)
    G["ffn_conv_w"] = jnp.concatenate([dwg[:FFN_CONV_WIDTH], dwv[:FFN_CONV_WIDTH]], axis=1)
    G["ffn_conv_b"] = jnp.concatenate([dbg, dbv], axis=1)
    zero = on_grad("w_up", _matmul(h2, du, "tn", "ffn_up_dw", out_shards=True))
    dh2 = _matmul(du, P["w_up"], "nt", "ffn_up_dx")
    dx1, dout, G["norm_ffn_pre"], G["norm_mix_post"] = _mid_bwd_call(x1, P["norm_ffn_pre"] + zero, dh2, dx2, out, P["norm_mix_post"])
    zero = on_grad("w_out", shard_major(_matmul(mixed, dout, "tn", "mix_out_dw")))
    dmixed = _matmul(dout, P["w_out"], "nt", "mix_out_dx")
    dya, dyc, dga, dgc, dba, dbc = _mix_bwd_call(dmixed, proj, gate_col0, P["b_gate"] + zero, y_a, y_c)
    G["b_gate"] = jnp.concatenate([dba, dbc], axis=1)
    zero = on_grad("w_attn_out", _matmul(a_bf, dya, "tn", "attn_out_dw", out_shards=True))
    zero = zero + on_grad("conv_pw_w", shard_major(_matmul(cact, dyc, "tn", "conv_pw_dw")))
    da = _matmul(dya, P["w_attn_out"], "nt", "attn_out_dx")
    dcact = _matmul(dyc, P["conv_pw_w"], "nt", "conv_pw_dx")
    dc1, G["conv_ln_g"], G["conv_ln_b"] = _ln_silu_bwd_call(c1, P["conv_ln_g"] + zero, P["conv_ln_b"], dcact)
    dval, dgate, dw_dw, G["conv_dw_b"] = _conv_bwd_call(proj, glu_col0, P["conv_dw_w"], dc1)
    G["conv_dw_w"] = dw_dw[:CONV_WIDTH]
    delta = _attn_delta_call(a, da)
    dqs, dks, dvs, dbs = [], [], [], []
    for g in range(N_GROUPS):
        dq, dk, dv, db = _attn_bwd_call(proj, bias, da, lse, delta, g)
        dqs.append(dq)
        dks.append(dk)
        dvs.append(dv)
        dbs.append(db)
    G["rel_bias"] = _bias_grad_call(jnp.concatenate(dbs, axis=0), buckets)
    dproj = _dproj_call(dqs + dks + dvs, [dval, dgate, dga, dgc])
    zero = on_grad("w_in", _matmul(h1, dproj, "tn", "proj_in_dw", out_shards=True))
    dh1 = _matmul(dproj, P["w_in"], "nt", "proj_in_dx")
    zero = zero + on_grad(None, dh1)
    grad_x, G["norm_mix_pre"] = _in_bwd_call(xs, P["norm_mix_pre"] + zero, dh1, dx1)
    return loss_tile, grad_x, G


def kernel(x, w_in, b_gate, rel_bias, w_attn_out, conv_dw_w, conv_dw_b, conv_ln_g, conv_ln_b, conv_pw_w, w_out, norm_mix_pre, norm_mix_post, norm_ffn_pre, norm_ffn_post, w_up, ffn_conv_w, ffn_conv_b, w_down, loss_target, m_w_in, m_b_gate, m_rel_bias, m_w_attn_out, m_conv_dw_w, m_conv_dw_b, m_conv_ln_g, m_conv_ln_b, m_conv_pw_w, m_w_out, m_norm_mix_pre, m_norm_mix_post, m_norm_ffn_pre, m_norm_ffn_post, m_w_up, m_ffn_conv_w, m_ffn_conv_b, m_w_down, v_w_in, v_b_gate, v_rel_bias, v_w_attn_out, v_conv_dw_w, v_conv_dw_b, v_conv_ln_g, v_conv_ln_b, v_conv_pw_w, v_w_out, v_norm_mix_pre, v_norm_mix_post, v_norm_ffn_pre, v_norm_ffn_post, v_w_up, v_ffn_conv_w, v_ffn_conv_b, v_w_down):
    weights = dict(w_in=w_in, b_gate=b_gate, rel_bias=rel_bias, w_attn_out=w_attn_out, conv_dw_w=conv_dw_w, conv_dw_b=conv_dw_b,
                   conv_ln_g=conv_ln_g, conv_ln_b=conv_ln_b, conv_pw_w=conv_pw_w, w_out=w_out, norm_mix_pre=norm_mix_pre,
                   norm_mix_post=norm_mix_post, norm_ffn_pre=norm_ffn_pre, norm_ffn_post=norm_ffn_post, w_up=w_up,
                   ffn_conv_w=ffn_conv_w, ffn_conv_b=ffn_conv_b, w_down=w_down)
    m_in = dict(w_in=m_w_in, b_gate=m_b_gate, rel_bias=m_rel_bias, w_attn_out=m_w_attn_out, conv_dw_w=m_conv_dw_w,
                conv_dw_b=m_conv_dw_b, conv_ln_g=m_conv_ln_g, conv_ln_b=m_conv_ln_b, conv_pw_w=m_conv_pw_w, w_out=m_w_out,
                norm_mix_pre=m_norm_mix_pre, norm_mix_post=m_norm_mix_post, norm_ffn_pre=m_norm_ffn_pre,
                norm_ffn_post=m_norm_ffn_post, w_up=m_w_up, ffn_conv_w=m_ffn_conv_w, ffn_conv_b=m_ffn_conv_b, w_down=m_w_down)
    v_in = dict(w_in=v_w_in, b_gate=v_b_gate, rel_bias=v_rel_bias, w_attn_out=v_w_attn_out, conv_dw_w=v_conv_dw_w,
                conv_dw_b=v_conv_dw_b, conv_ln_g=v_conv_ln_g, conv_ln_b=v_conv_ln_b, conv_pw_w=v_conv_pw_w, w_out=v_w_out,
                norm_mix_pre=v_norm_mix_pre, norm_mix_post=v_norm_mix_post, norm_ffn_pre=v_norm_ffn_pre,
                norm_ffn_post=v_norm_ffn_post, w_up=v_w_up, ffn_conv_w=v_ffn_conv_w, ffn_conv_b=v_ffn_conv_b, w_down=v_w_down)
    names = list(weights)
    xi, yi, ci = _position()
    chip = 2 * xi + yi
    core_arr = jnp.reshape(ci, (1,)).astype(jnp.int32)

    xs = x[0]
    target = loss_target[0]
    S, D = xs.shape

    big = ["w_in", "w_attn_out", "conv_pw_w", "w_out", "w_up", "w_down"]
    row_sharded = ("conv_pw_w", "w_out", "w_down")
    bf16_shard = {k: weights[k][0].astype(BF16) for k in big}
    natural = lambda k, g: g.reshape(-1, g.shape[2]) if k in row_sharded else g
    w_in_full, dw4, fc4 = _allgather_call([bf16_shard["w_in"]], [conv_dw_w[0], ffn_conv_w[0]])
    late_sets = dict(mix=["w_attn_out", "conv_pw_w", "w_out"], ffn=["w_up", "w_down"])
    late_groups = []
    for keys in late_sets.values():
        srcs = [bf16_shard[k] for k in keys]
        late_groups.append((srcs, [lax.empty((N_CHIPS,) + s.shape, BF16) for s in srcs], 4 * len(keys), _gather_copies))
    started = dict(zip(late_sets, _split_start("gather_late_start", late_groups, w_in_full)))
    launched = started["mix"]["token"]

    def late_weights(tag, after):
        landed = _split_wait(f"gather_{tag}_wait", started[tag], _gather_copies, after)[len(late_sets[tag]):]
        return {k: natural(k, g) for k, g in zip(late_sets[tag], landed)}

    chip_core = jnp.stack([chip, ci]).astype(jnp.int32)
    exchanging, pending = {}, {}

    def launch(tag, g3, after):
        keys, groups, partial = [], [], {}
        for k in list(exchanging):
            gk, r1 = _split_wait(f"sibling_exchange_wait_{k}", exchanging.pop(k), _sibling_copies, after)
            partial[k], s16 = _pair_sum_call(gk, r1, core_arr, f"pair_sum_{k}")
            keys.append(k)
            groups.append(([s16], [lax.empty((3,) + s16.shape[1:], BF16)], 3, _exchange_copies))
        if g3 is not None:
            groups.append(([g3], [lax.empty((N_CHIPS, g3.shape[1] // 2, g3.shape[2]), F32)], 1, _sibling_copies))
        begun = _split_start(f"grad_exchange_start_{tag}", groups, core_arr)
        for k, st in zip(keys, begun):
            pending[k] = (partial[k], st)
        if g3 is not None:
            exchanging[tag] = begun[-1]
        return begun[0]["token"]

    def on_grad(k, g3):
        if k is None:
            return launch("last", None, g3[:SUBLANES, :LANES])
        return launch(k, g3, g3[0, :SUBLANES, :LANES])

    def finish(keys, after, tag):
        halves = []
        for k in keys:
            s32, st = pending[k]
            recv2 = _split_wait(f"chip_exchange_wait_{k}", st, _exchange_copies, after)[1]
            halves.append(_chip_sum_call(s32, recv2, chip_core, f"chip_sum_{k}"))
        return dict(zip(keys, _sibling_assemble_call(halves, f"grad_sibling_assemble_{tag}")))

    P = dict(w_in=w_in_full, conv_dw_w=jnp.concatenate(list(dw4), axis=1), ffn_conv_w=jnp.concatenate(list(fc4), axis=1),
             b_gate=b_gate, rel_bias=rel_bias, conv_dw_b=conv_dw_b, conv_ln_g=conv_ln_g, conv_ln_b=conv_ln_b,
             norm_mix_pre=norm_mix_pre + launched, norm_mix_post=norm_mix_post, norm_ffn_pre=norm_ffn_pre,
             norm_ffn_post=norm_ffn_post, ffn_conv_b=ffn_conv_b)
    loss_tile, grad_x, G = _local_step(xs, target, P, late_weights, on_grad)

    small = [k for k in names if k not in big]
    packed = _pack([loss_tile[:1]] + [G[k] for k in small])
    (allsum,) = _split_start("small_grad_allsum_start",
                             [([packed], [jnp.zeros((N_DEVICES,) + packed.shape, F32)], N_DEVICES - 1, _allsum_copies)], core_arr)

    reduced, grads, deltas, new_m, new_v = {}, {}, {}, {}, {}

    def update(keys):
        for k in keys:
            d, mn, vn = _adamw_call(weights[k][0], reduced[k], m_in[k][0], v_in[k][0], f"adamw_{k}")
            grads[k], deltas[k], new_m[k], new_v[k] = reduced[k][None], d[None], mn[None], vn[None]

    others = [k for k in big if k != "w_in"]
    reduced.update(finish(others, allsum["tile"], "others"))
    update(others)
    reduced.update(finish(["w_in"], deltas["w_up"], "w_in"))
    update(["w_in"])

    me = jnp.reshape(4 * xi + 2 * yi + ci, (1,)).astype(jnp.int32)
    mine, landed = _split_wait("small_grad_allsum_wait", allsum, _allsum_copies, deltas["w_in"])
    summed_block = _ordered_sum_call(mine, landed, me)
    loss_row, *summed = _unpack(summed_block, [(1, LANES)] + [G[k].shape for k in small])
    loss = loss_row[0, 0]
    for k, gsum in zip(small, summed):
        if k in ("conv_dw_w", "ffn_conv_w"):
            cols = weights[k].shape[2]
            reduced[k] = lax.dynamic_slice_in_dim(gsum, chip * cols, cols, axis=1)
        else:
            reduced[k] = gsum
    flat2 = lambda t: t.reshape(-1, t.shape[-1]) if t.ndim == 3 else t
    ds, mns, vns = _adamw_small_call([flat2(weights[k]) for k in small], [reduced[k] for k in small],
                                     [flat2(m_in[k]) for k in small], [flat2(v_in[k]) for k in small])
    for k, dk_, mk, vk in zip(small, ds, mns, vns):
        shape = weights[k].shape
        grads[k], deltas[k], new_m[k], new_v[k] = reduced[k].reshape(shape), dk_.reshape(shape), mk.reshape(shape), vk.reshape(shape)

    return (loss, grad_x[None], *[grads[k] for k in names], *[deltas[k] for k in names],
            *[new_m[k] for k in names], *[new_v[k] for k in names])
```

```python
import functools
import math

import jax
import jax.numpy as jnp
import numpy as np
from jax import lax
from jax.experimental import pallas as pl
from jax.experimental.pallas import tpu as pltpu

F32 = jnp.float32
BF16 = jnp.bfloat16
MESH = pl.DeviceIdType.MESH

HEAD_DIM = 128
HEADS_PER_GROUP = 4
DILATED_PATTERNS = ((128, 1), (512, 4), (2048, 16))
N_GROUPS = 3
N_HEADS = N_GROUPS * HEADS_PER_GROUP
SPAN = 128
GROUP_WIDTH = HEADS_PER_GROUP * HEAD_DIM
CONV_WIDTH = 31
FFN_CONV_WIDTH = 3
N_BUCKETS = 32
MAX_DISTANCE = 2048
RMS_EPS = 1e-6
LN_EPS = 1e-5
NEG_INF = -1e30
ADAM_LR = 0.001
ADAM_B1 = 0.9
ADAM_B2 = 0.999
ADAM_EPS = 1e-08
ADAM_WD = 0.01
ADAM_STEP = 10

LANES = 128
SUBLANES = 8
ROW_TILE = 512
GATE_ROWS, GATE_COLS = 512, 512
TIME_BLOCK = 128
CONV_PAD = 32
FFN_PAD = 8
VMEM_LIMIT = 56 << 20


def _params(sem=None, vmem=None):
    kw = {}
    if sem is not None:
        kw["dimension_semantics"] = sem
    if vmem is not None:
        kw["vmem_limit_bytes"] = vmem
    return pltpu.CompilerParams(**kw)


def _pick(n, cands):
    for c in cands:
        if n % c == 0:
            return c
    return n


ELEMENTWISE_TILE_BYTES = 3 << 19


def _row_tile(rows, cols):
    for align in (16, SUBLANES):
        fits = [t for t in range(align, rows + 1, align) if rows % t == 0 and t * cols * 4 <= ELEMENTWISE_TILE_BYTES]
        if fits:
            return max(fits)
    return SUBLANES


N_CHIPS = 4
M_TILES = (1024, 1408, 512, 256, 128)
N_TILES = (1024, 512, 1408, 256, 128)
K_TILES = (2176, 2048, 1408, 1024, 512, 256, 128)


def _matmul(a, b, mode, name, out_shards=False, tm=None):
    assert a.dtype == BF16 and b.dtype == BF16, (name, a.dtype, b.dtype)
    b3 = b.ndim == 3
    tn = tk = None
    halves = None
    if mode == "nn":
        M, K = a.shape
        N = b.shape[-1] * (N_CHIPS if b3 else 1)
        tn = b.shape[-1] if b3 else None
    elif mode == "nt":
        if a.ndim == 3:
            halves = a.shape[2]
        M, K = a.shape[-2], a.shape[-1] * (a.shape[0] if a.ndim == 3 else 1)
        N = b.shape[-2]
        tk = b.shape[-1] if b3 else None
    else:
        if b3:
            halves = b.shape[2]
        K, M = a.shape
        N = b.shape[-1] * (b.shape[0] if b3 else 1)
        tn = N // N_CHIPS if out_shards else None
    tm = tm or _pick(M, M_TILES)
    tn = tn or _pick(N, N_TILES)
    tk = tk or _pick(K, K_TILES)
    nk = K // tk
    dn = {"nn": (((1,), (0,)), ((), ())), "nt": (((1,), (1,)), ((), ())), "tn": (((0,), (0,)), ((), ()))}[mode]

    def body(a_ref, b_ref, o_ref):
        if nk == 1:
            o_ref[...] = lax.dot_general(a_ref[...], b_ref[...], dn, preferred_element_type=F32)
        else:
            @pl.when(pl.program_id(2) == 0)
            def _():
                o_ref[...] = jnp.zeros_like(o_ref)

            o_ref[...] += lax.dot_general(a_ref[...], b_ref[...], dn, preferred_element_type=F32)

    if mode == "tn":
        a_spec = pl.BlockSpec((tk, tm), lambda i, j, k: (k, i))
    elif halves:
        per = halves // tk
        a_spec = pl.BlockSpec((None, tm, tk), lambda i, j, k: (k // per, i, k % per))
    else:
        a_spec = pl.BlockSpec((tm, tk), lambda i, j, k: (i, k))
    if mode == "nn":
        b_spec = pl.BlockSpec((None, tk, tn), lambda i, j, k: (j, k, 0)) if b3 else pl.BlockSpec((tk, tn), lambda i, j, k: (k, j))
    elif mode == "nt":
        b_spec = pl.BlockSpec((None, tn, tk), lambda i, j, k: (k, j, 0)) if b3 else pl.BlockSpec((tn, tk), lambda i, j, k: (j, k))
    elif halves:
        per = halves // tn
        b_spec = pl.BlockSpec((None, tk, tn), lambda i, j, k: (j // per, k, j % per))
    else:
        b_spec = pl.BlockSpec((tk, tn), lambda i, j, k: (k, j))
    if out_shards:
        out_spec = pl.BlockSpec((None, tm, tn), lambda i, j, k: (j, i, 0))
        out_shape = jax.ShapeDtypeStruct((N_CHIPS, M, tn), F32)
    else:
        out_spec = pl.BlockSpec((tm, tn), lambda i, j, k: (i, j))
        out_shape = jax.ShapeDtypeStruct((M, N), F32)
    return pl.pallas_call(
        body, name=name, grid=(M // tm, N // tn, nk),
        in_specs=[a_spec, b_spec], out_specs=out_spec, out_shape=out_shape,
        compiler_params=_params(("parallel", "parallel", "arbitrary"), VMEM_LIMIT),
    )(a, b)


def _rms(x, g):
    r = lax.rsqrt(jnp.mean(x * x, axis=-1, keepdims=True) + RMS_EPS)
    return x * r * g


def _rms_bwd(x, g, dy):
    r = lax.rsqrt(jnp.mean(x * x, axis=-1, keepdims=True) + RMS_EPS)
    n = x * r
    dn = dy * g
    dx = r * (dn - n * jnp.mean(dn * n, axis=-1, keepdims=True))
    return dx, jnp.sum(dy * n, axis=0, keepdims=True)


def _sigmoid(x):
    return 1.0 / (1.0 + jnp.exp(-x))


_GELU_C = math.sqrt(2.0 / math.pi)


def _gelu(x):
    return 0.5 * x * (1.0 + jnp.tanh(_GELU_C * (x + 0.044715 * x * x * x)))


def _gelu_and_grad(x):
    x2 = x * x
    t = jnp.tanh(_GELU_C * x * (1.0 + 0.044715 * x2))
    half = 0.5 * (1.0 + t)
    return x * half, half + (0.5 * _GELU_C) * x * (1.0 - t * t) * (1.0 + (3.0 * 0.044715) * x2)


def _row_spec(width, col_block=0):
    return pl.BlockSpec((ROW_TILE, width), lambda i: (i, col_block))


def _vec_spec(width, col_block=0):
    return pl.BlockSpec((1, width), lambda i: (0, col_block))


def _accumulate(ref, part):
    @pl.when(pl.program_id(0) == 0)
    def _():
        ref[...] = part

    @pl.when(pl.program_id(0) > 0)
    def _():
        ref[...] += part


def _rms_fwd_call(x, g):
    S, D = x.shape

    def body(x_ref, g_ref, h_ref):
        h_ref[...] = _rms(x_ref[...], g_ref[...]).astype(BF16)

    return pl.pallas_call(
        body, name="rms_mix_pre", grid=(S // ROW_TILE,),
        in_specs=[_row_spec(D), _vec_spec(D)], out_specs=_row_spec(D),
        out_shape=jax.ShapeDtypeStruct((S, D), BF16),
        compiler_params=_params(("parallel",)),
    )(x, g)


def _ln_silu_call(c1, g, b):
    S, C = c1.shape

    def body(c_ref, g_ref, b_ref, o_ref):
        xv = c_ref[...]
        mu = jnp.mean(xv, axis=-1, keepdims=True)
        xc = xv - mu
        var = jnp.mean(xc * xc, axis=-1, keepdims=True)
        z = xc * lax.rsqrt(var + LN_EPS) * g_ref[...] + b_ref[...]
        o_ref[...] = (z * _sigmoid(z)).astype(BF16)

    return pl.pallas_call(
        body, name="conv_ln_silu", grid=(S // ROW_TILE,),
        in_specs=[_row_spec(C), _vec_spec(C), _vec_spec(C)], out_specs=_row_spec(C),
        out_shape=jax.ShapeDtypeStruct((S, C), BF16),
        compiler_params=_params(("parallel",)),
    )(c1, g, b)


def _ln_silu_bwd_call(c1, g, b, dc):
    S, C = c1.shape

    def body(c_ref, g_ref, b_ref, dc_ref, dx_ref, dg_ref, db_ref):
        xv = c_ref[...]
        mu = jnp.mean(xv, axis=-1, keepdims=True)
        xc = xv - mu
        rs = lax.rsqrt(jnp.mean(xc * xc, axis=-1, keepdims=True) + LN_EPS)
        xh = xc * rs
        z = xh * g_ref[...] + b_ref[...]
        sg = _sigmoid(z)
        dz = dc_ref[...] * (sg * (1.0 + z * (1.0 - sg)))
        dxh = dz * g_ref[...]
        dx_ref[...] = rs * (dxh - jnp.mean(dxh, axis=-1, keepdims=True) - xh * jnp.mean(dxh * xh, axis=-1, keepdims=True))
        _accumulate(dg_ref, jnp.sum(dz * xh, axis=0, keepdims=True))
        _accumulate(db_ref, jnp.sum(dz, axis=0, keepdims=True))

    return pl.pallas_call(
        body, name="conv_ln_silu_bwd", grid=(S // ROW_TILE,),
        in_specs=[_row_spec(C), _vec_spec(C), _vec_spec(C), _row_spec(C)],
        out_specs=[_row_spec(C), _vec_spec(C), _vec_spec(C)],
        out_shape=[jax.ShapeDtypeStruct((S, C), F32), jax.ShapeDtypeStruct((1, C), F32), jax.ShapeDtypeStruct((1, C), F32)],
        compiler_params=_params(("arbitrary",)),
    )(c1, g, b, dc)


def _mix_call(proj, gate_col0, b_gate, y_a, y_c):
    S, D = y_a.shape
    w = GATE_COLS
    nc = D // w
    ga0, gc0 = gate_col0 // w, (gate_col0 + D) // w

    def body(ga_ref, gc_ref, ba_ref, bc_ref, ya_ref, yc_ref, o_ref):
        o_ref[...] = (_sigmoid(ga_ref[...] + ba_ref[...]) * ya_ref[...]
                      + _sigmoid(gc_ref[...] + bc_ref[...]) * yc_ref[...]).astype(BF16)

    tile = lambda off: pl.BlockSpec((GATE_ROWS, w), lambda i, j: (i, off + j))
    vec = lambda off: pl.BlockSpec((1, w), lambda i, j: (0, off + j))
    return pl.pallas_call(
        body, name="gate_mix", grid=(S // GATE_ROWS, nc),
        in_specs=[tile(ga0), tile(gc0), vec(0), vec(nc), tile(0), tile(0)],
        out_specs=tile(0), out_shape=jax.ShapeDtypeStruct((S, D), BF16),
        compiler_params=_params(("parallel", "parallel")),
    )(proj, proj, b_gate, b_gate, y_a, y_c)


def _mix_bwd_call(dmixed, proj, gate_col0, b_gate, y_a, y_c):
    S, D = y_a.shape
    w = GATE_COLS
    nc = D // w
    ga0, gc0 = gate_col0 // w, (gate_col0 + D) // w

    def body(dm_ref, ga_ref, gc_ref, ba_ref, bc_ref, ya_ref, yc_ref, dya_ref, dyc_ref, dga_ref, dgc_ref, dba_ref, dbc_ref):
        dm = dm_ref[...]
        sa = _sigmoid(ga_ref[...] + ba_ref[...])
        sc = _sigmoid(gc_ref[...] + bc_ref[...])
        dya_ref[...] = (dm * sa).astype(BF16)
        dyc_ref[...] = (dm * sc).astype(BF16)
        dga = dm * ya_ref[...] * sa * (1.0 - sa)
        dgc = dm * yc_ref[...] * sc * (1.0 - sc)
        dga_ref[...] = dga.astype(BF16)
        dgc_ref[...] = dgc.astype(BF16)
        pa = jnp.sum(dga, axis=0, keepdims=True)
        pc = jnp.sum(dgc, axis=0, keepdims=True)

        @pl.when(pl.program_id(1) == 0)
        def _():
            dba_ref[...] = pa
            dbc_ref[...] = pc

        @pl.when(pl.program_id(1) > 0)
        def _():
            dba_ref[...] += pa
            dbc_ref[...] += pc

    tile = lambda off: pl.BlockSpec((GATE_ROWS, w), lambda j, i: (i, off + j))
    vec = lambda off: pl.BlockSpec((1, w), lambda j, i: (0, off + j))
    return pl.pallas_call(
        body, name="gate_mix_bwd", grid=(nc, S // GATE_ROWS),
        in_specs=[tile(0), tile(ga0), tile(gc0), vec(0), vec(nc), tile(0), tile(0)],
        out_specs=[tile(0), tile(0), tile(0), tile(0), vec(0), vec(0)],
        out_shape=[jax.ShapeDtypeStruct((S, D), BF16)] * 4 + [
                   jax.ShapeDtypeStruct((1, D), F32), jax.ShapeDtypeStruct((1, D), F32)],
        compiler_params=_params(("parallel", "arbitrary")),
    )(dmixed, proj, proj, b_gate, b_gate, y_a, y_c)


def _res1_call(x, out, g_post, g_pre):
    S, D = x.shape

    def body(x_ref, o_ref, gp_ref, gq_ref, x1_ref, h2_ref):
        x1 = x_ref[...] + _rms(o_ref[...], gp_ref[...])
        x1_ref[...] = x1
        h2_ref[...] = _rms(x1, gq_ref[...]).astype(BF16)

    return pl.pallas_call(
        body, name="residual_mix", grid=(S // ROW_TILE,),
        in_specs=[_row_spec(D), _row_spec(D), _vec_spec(D), _vec_spec(D)],
        out_specs=[_row_spec(D), _row_spec(D)],
        out_shape=[jax.ShapeDtypeStruct((S, D), F32), jax.ShapeDtypeStruct((S, D), BF16)],
        compiler_params=_params(("parallel",)),
    )(x, out, g_post, g_pre)


def _loss_call(y, x1, g_post, target):
    S, D = y.shape

    def body(y_ref, x1_ref, g_ref, t_ref, loss_ref, dx_ref, dy_ref, dg_ref):
        yv, gv = y_ref[...], g_ref[...]
        err = x1_ref[...] + _rms(yv, gv) - t_ref[...]
        dx2 = err * (1.0 / D)
        dx_ref[...] = dx2
        dy, dg = _rms_bwd(yv, gv, dx2)
        dy_ref[...] = dy.astype(BF16)
        _accumulate(dg_ref, dg)
        part = 0.5 * jnp.sum(jnp.mean(err * err, axis=-1, keepdims=True), axis=0, keepdims=True)
        _accumulate(loss_ref, jnp.broadcast_to(part, (SUBLANES, LANES)))

    return pl.pallas_call(
        body, name="residual_ffn_loss", grid=(S // ROW_TILE,),
        in_specs=[_row_spec(D), _row_spec(D), _vec_spec(D), _row_spec(D)],
        out_specs=[pl.BlockSpec((SUBLANES, LANES), lambda i: (0, 0)), _row_spec(D), _row_spec(D), _vec_spec(D)],
        out_shape=[jax.ShapeDtypeStruct((SUBLANES, LANES), F32), jax.ShapeDtypeStruct((S, D), F32),
                   jax.ShapeDtypeStruct((S, D), BF16), jax.ShapeDtypeStruct((1, D), F32)],
        compiler_params=_params(("arbitrary",)),
    )(y, x1, g_post, target)


def _mid_bwd_call(x1, g_pre, dh2, dx2, out, g_post):
    S, D = x1.shape

    def body(x1_ref, gq_ref, dh_ref, dx2_ref, o_ref, gp_ref, dx1_ref, do_ref, dgq_ref, dgp_ref):
        d, dgq = _rms_bwd(x1_ref[...], gq_ref[...], dh_ref[...])
        dx1 = dx2_ref[...] + d
        dx1_ref[...] = dx1
        do, dgp = _rms_bwd(o_ref[...], gp_ref[...], dx1)
        do_ref[...] = do.astype(BF16)
        _accumulate(dgq_ref, dgq)
        _accumulate(dgp_ref, dgp)

    return pl.pallas_call(
        body, name="residual_mix_bwd", grid=(S // ROW_TILE,),
        in_specs=[_row_spec(D), _vec_spec(D), _row_spec(D), _row_spec(D), _row_spec(D), _vec_spec(D)],
        out_specs=[_row_spec(D), _row_spec(D), _vec_spec(D), _vec_spec(D)],
        out_shape=[jax.ShapeDtypeStruct((S, D), F32), jax.ShapeDtypeStruct((S, D), BF16)] + [jax.ShapeDtypeStruct((1, D), F32)] * 2,
        compiler_params=_params(("arbitrary",)),
    )(x1, g_pre, dh2, dx2, out, g_post)


def _in_bwd_call(x, g, dh1, dx1):
    S, D = x.shape

    def body(x_ref, g_ref, dh_ref, dx1_ref, gx_ref, dg_ref):
        d, dg = _rms_bwd(x_ref[...], g_ref[...], dh_ref[...])
        gx_ref[...] = dx1_ref[...] + d
        _accumulate(dg_ref, dg)

    return pl.pallas_call(
        body, name="rms_mix_pre_bwd", grid=(S // ROW_TILE,),
        in_specs=[_row_spec(D), _vec_spec(D), _row_spec(D), _row_spec(D)],
        out_specs=[_row_spec(D), _vec_spec(D)],
        out_shape=[jax.ShapeDtypeStruct((S, D), F32), jax.ShapeDtypeStruct((1, D), F32)],
        compiler_params=_params(("arbitrary",)),
    )(x, g, dh1, dx1)


def _bucket_table(dilation):
    qi = np.arange(SPAN)[:, None]
    ki = np.arange(2 * SPAN)[None, :]
    dist = np.maximum(qi + SPAN - ki, 0) * dilation
    max_exact = N_BUCKETS // 2
    d = np.maximum(dist, 1).astype(np.float64)
    large = max_exact + (np.log(d / max_exact) / math.log(MAX_DISTANCE / max_exact) * (N_BUCKETS - max_exact)).astype(np.int32)
    large = np.minimum(large, N_BUCKETS - 1)
    return np.where(dist < max_exact, dist, large).astype(np.int32)


def _bucket_tables():
    return jnp.asarray(np.stack([_bucket_table(r) for _, r in DILATED_PATTERNS]))


def _bias_table_call(rel_bias, buckets):
    def body(rb_ref, bk_ref, o_ref):
        for h in range(N_HEADS):
            bk = bk_ref[h // HEADS_PER_GROUP]

            def step(b, acc):
                return jnp.where(bk == b, rb_ref[b, h], acc)

            o_ref[h] = lax.fori_loop(0, N_BUCKETS, step, jnp.zeros((SPAN, 2 * SPAN), F32))

    return pl.pallas_call(
        body, name="rel_bias_table",
        in_specs=[pl.BlockSpec(memory_space=pltpu.SMEM), pl.BlockSpec(memory_space=pltpu.VMEM)],
        out_specs=pl.BlockSpec(memory_space=pltpu.VMEM),
        out_shape=jax.ShapeDtypeStruct((N_HEADS, SPAN, 2 * SPAN), F32),
    )(rel_bias, buckets)


def _bias_grad_call(dbias, buckets):
    def body(db_ref, bk_ref, o_ref, rows_ref):
        for h in range(N_HEADS):
            bk = bk_ref[h // HEADS_PER_GROUP]
            dv = db_ref[h]

            def step(b, carry):
                rows_ref[h, b] = jnp.sum(jnp.where(bk == b, dv, 0.0), axis=0, keepdims=True)
                return carry

            lax.fori_loop(0, N_BUCKETS, step, 0)
        o_ref[...] = jnp.sum(rows_ref[...], axis=-1, keepdims=True)

    out = pl.pallas_call(
        body, name="rel_bias_grad",
        in_specs=[pl.BlockSpec(memory_space=pltpu.VMEM), pl.BlockSpec(memory_space=pltpu.VMEM)],
        out_specs=pl.BlockSpec(memory_space=pltpu.VMEM),
        out_shape=jax.ShapeDtypeStruct((N_HEADS, N_BUCKETS, 1, 1), F32),
        scratch_shapes=[pltpu.VMEM((N_HEADS, N_BUCKETS, 1, 2 * SPAN), F32)],
    )(dbias, buckets)
    return out.reshape(N_HEADS, N_BUCKETS).T


def _dot_nt(a, b):
    return lax.dot_general(a, b, (((1,), (1,)), ((), ())), preferred_element_type=F32)


def _dot_nn(a, b):
    return lax.dot_general(a, b, (((1,), (0,)), ((), ())), preferred_element_type=F32)


def _dot_tn(a, b):
    return lax.dot_general(a, b, (((0,), (0,)), ((), ())), preferred_element_type=F32)


def _band_masks(n, nb):
    qi = lax.broadcasted_iota(jnp.int32, (SPAN, SPAN), 0)
    ki = lax.broadcasted_iota(jnp.int32, (SPAN, SPAN), 1)
    prev_ok = jnp.logical_and(ki >= qi, n > 0)
    cur_ok = ki <= qi
    next_ok = jnp.logical_and(ki >= qi, n < nb - 1)
    return prev_ok, cur_ok, next_ok


def _wide_band_mask(n):
    qi = lax.broadcasted_iota(jnp.int32, (SPAN, 2 * SPAN), 0)
    ki = lax.broadcasted_iota(jnp.int32, (SPAN, 2 * SPAN), 1)
    prev_ok = jnp.logical_and(jnp.logical_and(ki < SPAN, ki >= qi), n > 0)
    cur_ok = jnp.logical_and(ki >= SPAN, ki - SPAN <= qi)
    return jnp.logical_or(prev_ok, cur_ok)


def _attn_plan(S, group):
    r = DILATED_PATTERNS[group][1]
    hp, per = (HEADS_PER_GROUP, 1) if r == 1 else (2, 4)
    return r, S // (r * SPAN), hp, per


def _residue_rows(rho, r):
    return slice(None) if r == 1 else pl.ds(rho, SPAN, stride=r)


def _for_residues(r, per, fn):
    if r == per:
        for u in range(per):
            fn(u)
        return

    def step(i, carry):
        for u in range(per):
            fn(i * per + u)
        return carry

    lax.fori_loop(0, r // per, step, 0)


def _attn_fwd_call(proj, bias, group):
    S = proj.shape[0]
    r, nb, hp, per = _attn_plan(S, group)
    scale = HEAD_DIM ** -0.5
    kinds = ("q", "kp", "kc", "vp", "vc") if nb > 1 else ("q", "kc", "vc")

    def body(*refs):
        ins = {kind: refs[i * hp:(i + 1) * hp] for i, kind in enumerate(kinds)}
        b_ref, o_ref, lse_ref = refs[len(kinds) * hp:]
        n = pl.program_id(1)
        prev_ok, cur_ok, _ = _band_masks(n, nb)

        band_ok = _wide_band_mask(n) if nb > 1 else cur_ok

        def residue(rho):
            rows = _residue_rows(rho, r)
            for j in range(hp):
                get = lambda kind: ins[kind][j][rows, :].astype(BF16)
                q = get("q")
                if nb > 1:
                    keys, vals, bias_j = jnp.concatenate([get("kp"), get("kc")], axis=0), jnp.concatenate([get("vp"), get("vc")], axis=0), b_ref[j]
                else:
                    keys, vals, bias_j = get("kc"), get("vc"), b_ref[j, :, SPAN:]
                s = jnp.where(band_ok, _dot_nt(q, keys) * scale + bias_j, NEG_INF)
                m = jnp.max(s, axis=-1, keepdims=True)
                p = jnp.exp(s - m)
                den = jnp.sum(p, axis=-1, keepdims=True)
                o_ref[j, rows, :] = _dot_nn(p.astype(BF16), vals) / den
                lse_ref[j, rows, :] = jnp.broadcast_to(m + jnp.log(den), (SPAN, HEAD_DIM))

        _for_residues(r, per, residue)

    in_specs = [_head_spec(r, nb, hp, kind, group, jj) for kind in kinds for jj in range(hp)]
    in_specs.append(pl.BlockSpec((hp, SPAN, 2 * SPAN), lambda j, n: (group * (HEADS_PER_GROUP // hp) + j, 0, 0)))
    out = pl.BlockSpec((hp, r * SPAN, HEAD_DIM), lambda j, n: (j, n, 0))
    return pl.pallas_call(
        body, name=f"attn_fwd_g{group}", grid=(HEADS_PER_GROUP // hp, nb),
        in_specs=in_specs, out_specs=[out] * 2,
        out_shape=[jax.ShapeDtypeStruct((HEADS_PER_GROUP, S, HEAD_DIM), F32)] * 2,
        compiler_params=_params(("parallel", "parallel"), VMEM_LIMIT),
    )(*([proj] * (len(in_specs) - 1)), bias)


_PROJ_PART = dict(q=0, qn=0, kp=1, kc=1, vp=2, vc=2)


def _head_spec(r, nb, hp, kind, group, jj):
    if kind in _PROJ_PART:
        base = (_PROJ_PART[kind] * N_GROUPS + group) * HEADS_PER_GROUP
    else:
        base = 0
    if kind.endswith("p"):
        row = lambda n: jnp.maximum(n - 1, 0)
    elif kind.endswith("n"):
        row = lambda n: jnp.minimum(n + 1, nb - 1)
    else:
        row = lambda n: n
    return pl.BlockSpec((r * SPAN, HEAD_DIM), lambda j, n: (row(n), base + j * hp + jj))


def _attn_merge_call(parts):
    S = parts[0].shape[1]

    def body(o1, s1, o2, s2, o3, s3, a_ref, ab_ref, lse_ref):
        for j in range(HEADS_PER_GROUP):
            sl = slice(j * HEAD_DIM, (j + 1) * HEAD_DIM)
            mx = jnp.maximum(jnp.maximum(s1[j], s2[j]), s3[j])
            w1 = jnp.exp(s1[j] - mx)
            w2 = jnp.exp(s2[j] - mx)
            w3 = jnp.exp(s3[j] - mx)
            den = w1 + w2 + w3
            a = (w1 * o1[j] + w2 * o2[j] + w3 * o3[j]) / den
            a_ref[:, sl] = a
            ab_ref[:, sl] = a.astype(BF16)
            lse_ref[:, sl] = mx + jnp.log(den)

    heads = pl.BlockSpec((HEADS_PER_GROUP, ROW_TILE, HEAD_DIM), lambda i: (0, i, 0))
    return pl.pallas_call(
        body, name="attn_merge", grid=(S // ROW_TILE,),
        in_specs=[heads] * 6, out_specs=[_row_spec(GROUP_WIDTH)] * 3,
        out_shape=[jax.ShapeDtypeStruct((S, GROUP_WIDTH), F32), jax.ShapeDtypeStruct((S, GROUP_WIDTH), BF16),
                   jax.ShapeDtypeStruct((S, GROUP_WIDTH), F32)],
        compiler_params=_params(("parallel",)),
    )(*parts)


def _attn_delta_call(a, da):
    S = a.shape[0]

    def body(a_ref, da_ref, d_ref):
        for j in range(HEADS_PER_GROUP):
            sl = slice(j * HEAD_DIM, (j + 1) * HEAD_DIM)
            d = jnp.sum(a_ref[:, sl] * da_ref[:, sl], axis=-1, keepdims=True)
            d_ref[:, sl] = jnp.broadcast_to(d, (ROW_TILE, HEAD_DIM))

    return pl.pallas_call(
        body, name="attn_delta", grid=(S // ROW_TILE,),
        in_specs=[_row_spec(GROUP_WIDTH)] * 2, out_specs=_row_spec(GROUP_WIDTH),
        out_shape=jax.ShapeDtypeStruct((S, GROUP_WIDTH), F32),
        compiler_params=_params(("parallel",)),
    )(a, da)


def _attn_bwd_call(proj, bias, da, lse, delta, group):
    S = proj.shape[0]
    r, nb, hp, per = _attn_plan(S, group)
    scale = HEAD_DIM ** -0.5
    kinds = ("q", "qn", "kp", "kc", "vp", "vc", "da", "dan", "lse", "lsen", "dl", "dln") if nb > 1 else ("q", "kc", "vc", "da", "lse", "dl")
    source = dict(da=da, dan=da, lse=lse, lsen=lse, dl=delta, dln=delta)

    def body(*refs):
        ins = {kind: refs[i * hp:(i + 1) * hp] for i, kind in enumerate(kinds)}
        b_ref, dq_ref, dk_ref, dv_ref, db_ref = refs[len(kinds) * hp:]
        n = pl.program_id(1)
        prev_ok, cur_ok, next_ok = _band_masks(n, nb)

        @pl.when(n == 0)
        def _():
            db_ref[...] = jnp.zeros_like(db_ref)

        band_ok = _wide_band_mask(n) if nb > 1 else cur_ok

        def residue(rho):
            rows = _residue_rows(rho, r)
            for j in range(hp):
                get = lambda kind: ins[kind][j][rows, :]
                q = get("q").astype(BF16)
                kc = get("kc").astype(BF16)
                vc = get("vc").astype(BF16)
                dav = get("da").astype(BF16)
                lse_q, dl_q = get("lse"), get("dl")
                if nb == 1:
                    pc = jnp.exp(jnp.where(cur_ok, _dot_nt(q, kc) * scale + b_ref[j, :, SPAN:], NEG_INF) - lse_q)
                    dsc = pc * (_dot_nt(dav, vc) - dl_q)
                    dsc_b = dsc.astype(BF16)
                    dq = _dot_nn(dsc_b, kc)
                    dk = _dot_tn(dsc_b, q)
                    dv = _dot_tn(pc.astype(BF16), dav)
                    db_ref[j, :, SPAN:] += dsc
                else:
                    qn = get("qn").astype(BF16)
                    dan = get("dan").astype(BF16)
                    keys = jnp.concatenate([get("kp").astype(BF16), kc], axis=0)
                    vals = jnp.concatenate([get("vp").astype(BF16), vc], axis=0)
                    wide = lambda t: jnp.concatenate([t, t], axis=1)
                    p = jnp.exp(jnp.where(band_ok, _dot_nt(q, keys) * scale + b_ref[j], NEG_INF) - wide(lse_q))
                    ds = p * (_dot_nt(dav, vals) - wide(dl_q))
                    dq = _dot_nn(ds.astype(BF16), keys)
                    db_ref[j] += ds
                    pn = jnp.exp(jnp.where(next_ok, _dot_nt(qn, kc) * scale + b_ref[j, :, :SPAN], NEG_INF) - get("lsen"))
                    dsn = pn * (_dot_nt(dan, vc) - get("dln"))
                    both = lambda cur_part, next_part: jnp.concatenate([cur_part.astype(BF16), next_part.astype(BF16)], axis=0)
                    dk = _dot_tn(both(ds[:, SPAN:], dsn), jnp.concatenate([q, qn], axis=0))
                    dv = _dot_tn(both(p[:, SPAN:], pn), jnp.concatenate([dav, dan], axis=0))
                dq_ref[j, rows, :] = dq * scale
                dk_ref[j, rows, :] = dk * scale
                dv_ref[j, rows, :] = dv

        _for_residues(r, per, residue)

    per_group = HEADS_PER_GROUP // hp
    band = (hp, SPAN, 2 * SPAN)
    in_specs = [_head_spec(r, nb, hp, kind, group, jj) for kind in kinds for jj in range(hp)]
    in_specs.append(pl.BlockSpec(band, lambda j, n: (group * per_group + j, 0, 0)))
    operands = [source.get(kind, proj) for kind in kinds for _ in range(hp)] + [bias]
    out = pl.BlockSpec((hp, r * SPAN, HEAD_DIM), lambda j, n: (j, n, 0))
    return pl.pallas_call(
        body, name=f"attn_bwd_g{group}", grid=(per_group, nb),
        in_specs=in_specs,
        out_specs=[out] * 3 + [pl.BlockSpec(band, lambda j, n: (j, 0, 0))],
        out_shape=[jax.ShapeDtypeStruct((HEADS_PER_GROUP, S, HEAD_DIM), F32)] * 3
        + [jax.ShapeDtypeStruct((HEADS_PER_GROUP, SPAN, 2 * SPAN), F32)],
        compiler_params=_params(("parallel", "arbitrary"), VMEM_LIMIT),
    )(*operands)


def _dproj_call(dqkv, tails):
    S = tails[0].shape[0]
    width = len(dqkv) * GROUP_WIDTH + sum(t.shape[1] for t in tails)

    def body(*refs):
        o_ref = refs[-1]
        col = 0
        for ref in refs[:len(dqkv)]:
            for j in range(HEADS_PER_GROUP):
                o_ref[:, col:col + HEAD_DIM] = ref[j].astype(BF16)
                col += HEAD_DIM
        for ref in refs[len(dqkv):-1]:
            o_ref[:, col:col + ref.shape[1]] = ref[...]
            col += ref.shape[1]

    heads = pl.BlockSpec((HEADS_PER_GROUP, ROW_TILE, HEAD_DIM), lambda i: (0, i, 0))
    return pl.pallas_call(
        body, name="dproj_assemble", grid=(S // ROW_TILE,),
        in_specs=[heads] * len(dqkv) + [_row_spec(t.shape[1]) for t in tails],
        out_specs=_row_spec(width), out_shape=jax.ShapeDtypeStruct((S, width), BF16),
        compiler_params=_params(("parallel",)),
    )(*dqkv, *tails)


def _tap_rows(xpad_ref, t0, k, width, pad):
    return xpad_ref[pl.ds(t0 + (pad - (width - 1 - k)), TIME_BLOCK), :]


def _conv_block(xpad_ref, t0, w_ref, width, pad):
    acc = None
    for k in range(width):
        term = w_ref[k:k + 1, :] * _tap_rows(xpad_ref, t0, k, width, pad)
        acc = term if acc is None else acc + term
    return acc


def _conv_transpose_block(dpad_ref, t0, w_ref, width):
    acc = None
    for k in range(width):
        term = w_ref[k:k + 1, :] * dpad_ref[pl.ds(t0 + (width - 1 - k), TIME_BLOCK), :]
        acc = term if acc is None else acc + term
    return acc


def _conv_weight_grad(xpad_ref, t0, dy, dw_ref, width, pad):
    for k in range(width):
        dw_ref[k:k + 1, :] += jnp.sum(dy * _tap_rows(xpad_ref, t0, k, width, pad), axis=0, keepdims=True)


def _time_loop(S, step):
    def it(tb, carry):
        step(pl.multiple_of(tb * TIME_BLOCK, TIME_BLOCK))
        return carry

    lax.fori_loop(0, S // TIME_BLOCK, it, 0)


def _conv_fwd_call(proj, col0, w, b):
    S = proj.shape[0]
    C = w.shape[1]
    nt = C // LANES
    v0, g0 = col0 // LANES, (col0 + C) // LANES

    def body(val_ref, gate_ref, w_ref, b_ref, o_ref, pad_ref):
        pad_ref[0:CONV_PAD, :] = jnp.zeros((CONV_PAD, LANES), F32)
        pad_ref[CONV_PAD:, :] = val_ref[...] * _sigmoid(gate_ref[...])

        def step(t0):
            o_ref[pl.ds(t0, TIME_BLOCK), :] = _conv_block(pad_ref, t0, w_ref, CONV_WIDTH, CONV_PAD) + b_ref[...]

        _time_loop(S, step)

    seq = lambda off: pl.BlockSpec((S, LANES), lambda i: (0, off + i))
    return pl.pallas_call(
        body, name="conv_module", grid=(nt,),
        in_specs=[seq(v0), seq(g0), pl.BlockSpec((CONV_WIDTH, LANES), lambda i: (0, i)), pl.BlockSpec((1, LANES), lambda i: (0, i))],
        out_specs=seq(0), out_shape=jax.ShapeDtypeStruct((S, C), F32),
        scratch_shapes=[pltpu.VMEM((S + CONV_PAD, LANES), F32)],
        compiler_params=_params(("parallel",)),
    )(proj, proj, w, b)


def _conv_bwd_call(proj, col0, w, dc1):
    S = proj.shape[0]
    C = w.shape[1]
    nt = C // LANES
    v0, g0 = col0 // LANES, (col0 + C) // LANES

    def body(val_ref, gate_ref, w_ref, dy_ref, dval_ref, dgate_ref, dw_ref, db_ref, xpad_ref, dpad_ref, dwacc_ref):
        xpad_ref[0:CONV_PAD, :] = jnp.zeros((CONV_PAD, LANES), F32)
        xpad_ref[CONV_PAD:, :] = val_ref[...] * _sigmoid(gate_ref[...])
        dpad_ref[0:S, :] = dy_ref[...]
        dpad_ref[S:, :] = jnp.zeros((CONV_PAD, LANES), F32)
        dwacc_ref[...] = jnp.zeros_like(dwacc_ref)

        def step(t0):
            rows = pl.ds(t0, TIME_BLOCK)
            _conv_weight_grad(xpad_ref, t0, dy_ref[rows, :], dwacc_ref, CONV_WIDTH, CONV_PAD)
            dc0 = _conv_transpose_block(dpad_ref, t0, w_ref, CONV_WIDTH)
            sg = _sigmoid(gate_ref[rows, :])
            dval_ref[rows, :] = (dc0 * sg).astype(BF16)
            dgate_ref[rows, :] = (dc0 * val_ref[rows, :] * sg * (1.0 - sg)).astype(BF16)

        _time_loop(S, step)
        dw_ref[...] = dwacc_ref[...]
        db_ref[...] = jnp.sum(dy_ref[...], axis=0, keepdims=True)

    seq = lambda off: pl.BlockSpec((S, LANES), lambda i: (0, off + i))
    return pl.pallas_call(
        body, name="conv_module_bwd", grid=(nt,),
        in_specs=[seq(v0), seq(g0), pl.BlockSpec((CONV_WIDTH, LANES), lambda i: (0, i)), seq(0)],
        out_specs=[seq(0), seq(0), pl.BlockSpec((CONV_PAD, LANES), lambda i: (0, i)), pl.BlockSpec((1, LANES), lambda i: (0, i))],
        out_shape=[jax.ShapeDtypeStruct((S, C), BF16), jax.ShapeDtypeStruct((S, C), BF16),
                   jax.ShapeDtypeStruct((CONV_PAD, C), F32), jax.ShapeDtypeStruct((1, C), F32)],
        scratch_shapes=[pltpu.VMEM((S + CONV_PAD, LANES), F32), pltpu.VMEM((S + CONV_PAD, LANES), F32),
                        pltpu.VMEM((CONV_PAD, LANES), F32)],
        compiler_params=_params(("parallel",)),
    )(proj, proj, w, dc1)


def _ffn_fwd_call(u, w, b):
    S, C2 = u.shape
    C = C2 // 2
    nt = C // LANES

    def body(ug_ref, uv_ref, wg_ref, wv_ref, bg_ref, bv_ref, f_ref, pg_ref, pv_ref):
        zeros = jnp.zeros((FFN_PAD, LANES), F32)
        pg_ref[0:FFN_PAD, :] = zeros
        pv_ref[0:FFN_PAD, :] = zeros
        pg_ref[FFN_PAD:, :] = ug_ref[...]
        pv_ref[FFN_PAD:, :] = uv_ref[...]

        def step(t0):
            cg = _conv_block(pg_ref, t0, wg_ref, FFN_CONV_WIDTH, FFN_PAD) + bg_ref[...]
            cv = _conv_block(pv_ref, t0, wv_ref, FFN_CONV_WIDTH, FFN_PAD) + bv_ref[...]
            f_ref[pl.ds(t0, TIME_BLOCK), :] = (_gelu(cg) * cv).astype(BF16)

        _time_loop(S, step)

    seq = lambda off: pl.BlockSpec((S, LANES), lambda i: (0, off + i))
    wsp = lambda off: pl.BlockSpec((FFN_CONV_WIDTH, LANES), lambda i: (0, off + i))
    bsp = lambda off: pl.BlockSpec((1, LANES), lambda i: (0, off + i))
    return pl.pallas_call(
        body, name="ffn_conv_geglu", grid=(nt,),
        in_specs=[seq(0), seq(nt), wsp(0), wsp(nt), bsp(0), bsp(nt)],
        out_specs=seq(0), out_shape=jax.ShapeDtypeStruct((S, C), BF16),
        scratch_shapes=[pltpu.VMEM((S + FFN_PAD, LANES), F32)] * 2,
        compiler_params=_params(("parallel",)),
    )(u, u, w, w, b, b)


def _ffn_bwd_call(u, w, b, df):
    S, C2 = u.shape
    C = C2 // 2
    nt = C // LANES

    def body(ug_ref, uv_ref, wg_ref, wv_ref, bg_ref, bv_ref, df_ref,
             du_ref, dwg_ref, dwv_ref, dbg_ref, dbv_ref,
             pg_ref, pv_ref, dg_ref, dv_ref, dwg_acc, dwv_acc, dbg_acc, dbv_acc):
        zeros = jnp.zeros((FFN_PAD, LANES), F32)
        pg_ref[0:FFN_PAD, :] = zeros
        pv_ref[0:FFN_PAD, :] = zeros
        pg_ref[FFN_PAD:, :] = ug_ref[...]
        pv_ref[FFN_PAD:, :] = uv_ref[...]
        dg_ref[S:, :] = zeros
        dv_ref[S:, :] = zeros
        dwg_acc[...] = jnp.zeros_like(dwg_acc)
        dwv_acc[...] = jnp.zeros_like(dwv_acc)
        dbg_acc[...] = jnp.zeros_like(dbg_acc)
        dbv_acc[...] = jnp.zeros_like(dbv_acc)

        def first(t0):
            rows = pl.ds(t0, TIME_BLOCK)
            cg = _conv_block(pg_ref, t0, wg_ref, FFN_CONV_WIDTH, FFN_PAD) + bg_ref[...]
            cv = _conv_block(pv_ref, t0, wv_ref, FFN_CONV_WIDTH, FFN_PAD) + bv_ref[...]
            dfb = df_ref[rows, :]
            gelu, gelu_grad = _gelu_and_grad(cg)
            dcg = dfb * cv * gelu_grad
            dcv = dfb * gelu
            dg_ref[rows, :] = dcg
            dv_ref[rows, :] = dcv
            _conv_weight_grad(pg_ref, t0, dcg, dwg_acc, FFN_CONV_WIDTH, FFN_PAD)
            _conv_weight_grad(pv_ref, t0, dcv, dwv_acc, FFN_CONV_WIDTH, FFN_PAD)
            dbg_acc[...] += jnp.sum(dcg, axis=0, keepdims=True)
            dbv_acc[...] += jnp.sum(dcv, axis=0, keepdims=True)

        def second(t0):
            rows = pl.ds(t0, TIME_BLOCK)
            du_ref[0, rows, :] = _conv_transpose_block(dg_ref, t0, wg_ref, FFN_CONV_WIDTH).astype(BF16)
            du_ref[1, rows, :] = _conv_transpose_block(dv_ref, t0, wv_ref, FFN_CONV_WIDTH).astype(BF16)

        _time_loop(S, first)
        _time_loop(S, second)
        dwg_ref[...] = dwg_acc[...]
        dwv_ref[...] = dwv_acc[...]
        dbg_ref[...] = dbg_acc[...]
        dbv_ref[...] = dbv_acc[...]

    seq = lambda off: pl.BlockSpec((S, LANES), lambda i: (0, off + i))
    wsp = lambda off: pl.BlockSpec((FFN_CONV_WIDTH, LANES), lambda i: (0, off + i))
    bsp = lambda off: pl.BlockSpec((1, LANES), lambda i: (0, off + i))
    return pl.pallas_call(
        body, name="ffn_conv_geglu_bwd", grid=(nt,),
        in_specs=[seq(0), seq(nt), wsp(0), wsp(nt), bsp(0), bsp(nt), seq(0)],
        out_specs=[pl.BlockSpec((2, S, LANES), lambda i: (0, 0, i)),
                   pl.BlockSpec((SUBLANES, LANES), lambda i: (0, i)), pl.BlockSpec((SUBLANES, LANES), lambda i: (0, i)),
                   bsp(0), bsp(0)],
        out_shape=[jax.ShapeDtypeStruct((2, S, C), BF16)] + [jax.ShapeDtypeStruct((SUBLANES, C), F32)] * 2
        + [jax.ShapeDtypeStruct((1, C), F32)] * 2,
        scratch_shapes=[pltpu.VMEM((S + FFN_PAD, LANES), F32)] * 4 + [pltpu.VMEM((SUBLANES, LANES), F32)] * 2
        + [pltpu.VMEM((1, LANES), F32)] * 2,
        compiler_params=_params(("parallel",)),
    )(u, u, w, w, b, b, df)


def _adamw(w_ref, g_ref, m_ref, v_ref, d_ref, mo_ref, vo_ref):
    gv = g_ref[...]
    mn = ADAM_B1 * m_ref[...] + (1.0 - ADAM_B1) * gv
    vn = ADAM_B2 * v_ref[...] + (1.0 - ADAM_B2) * (gv * gv)
    mo_ref[...] = mn
    vo_ref[...] = vn
    m_hat = mn * (1.0 / (1.0 - ADAM_B1 ** ADAM_STEP))
    v_hat = vn * (1.0 / (1.0 - ADAM_B2 ** ADAM_STEP))
    d_ref[...] = -ADAM_LR * (m_hat / (jnp.sqrt(v_hat) + ADAM_EPS) + ADAM_WD * w_ref[...])


def _adamw_call(w, g, m, v, name):
    R, C = w.shape
    tr = _row_tile(R, C)
    spec = pl.BlockSpec((tr, C), lambda i: (i, 0))
    return pl.pallas_call(
        _adamw_body(), name=name, grid=(R // tr,),
        in_specs=[spec] * 4, out_specs=[spec] * 3,
        out_shape=[jax.ShapeDtypeStruct((R, C), F32)] * 3,
        compiler_params=_params(("parallel",)),
    )(w, g, m, v)


def _adamw_body():
    def body(*refs):
        _adamw(*refs)

    return body


def _adamw_small_call(ws, gs, ms, vs):
    n = len(ws)

    def body(*refs):
        w_refs, g_refs, m_refs, v_refs, d_refs, mo_refs, vo_refs = (refs[i * n:(i + 1) * n] for i in range(7))
        for i in range(n):
            _adamw(w_refs[i], g_refs[i], m_refs[i], v_refs[i], d_refs[i], mo_refs[i], vo_refs[i])

    whole = pl.BlockSpec(memory_space=pltpu.VMEM)
    outs = pl.pallas_call(
        body, name="adamw_small",
        in_specs=[whole] * (4 * n), out_specs=[whole] * (3 * n),
        out_shape=[jax.ShapeDtypeStruct(w.shape, F32) for w in ws] * 3,
    )(*ws, *gs, *ms, *vs)
    return outs[:n], outs[n:2 * n], outs[2 * n:]


def _position():
    return lax.axis_index("x"), lax.axis_index("y"), lax.axis_index("c")


def _chip_peers(x, y):
    return [(x, 1 - y), (1 - x, y), (1 - x, 1 - y)]


def _half_rows(ref, core, rows):
    h = rows // 2
    start = pl.multiple_of(core * h, 16)
    return ref.at[pl.ds(start, h), :] if len(ref.shape) == 2 else ref.at[:, pl.ds(start, h), :]


def _shard_half(ref, shard, core, rows):
    h = rows // 2
    return ref.at[shard, pl.ds(pl.multiple_of(core * h, 16), h), :]


ANY = pl.BlockSpec(memory_space=pl.ANY)


def _allgather_call(shards, whole):
    n, nw = len(shards), len(whole)
    outs_shape = [jax.ShapeDtypeStruct((N_CHIPS,) + s.shape, s.dtype) for s in shards + whole]

    def body(*refs):
        ins, outs = refs[:n + nw], refs[n + nw:2 * (n + nw)]
        send_sems, recv_sems, pass_send, pass_recv, own_send, own_recv = refs[2 * (n + nw):]
        x, y, c = _position()
        chip = 2 * x + y
        peers = _chip_peers(x, y)
        sent, local = [], []
        for i in range(n + nw):
            cp = pltpu.make_async_remote_copy(src_ref=ins[i], dst_ref=outs[i].at[chip], send_sem=own_send.at[i],
                                              recv_sem=own_recv.at[i], device_id=(x, y, 1 - c), device_id_type=MESH)
            cp.start()
            local.append(cp)
            rows = ins[i].shape[0]
            for k, (px, py) in enumerate(peers):
                if i < n:
                    src, dst = _half_rows(ins[i], c, rows), _shard_half(outs[i], chip, c, rows)
                else:
                    src, dst = ins[i], outs[i].at[chip]
                cp = pltpu.make_async_remote_copy(src_ref=src, dst_ref=dst, send_sem=send_sems.at[i, k],
                                                  recv_sem=recv_sems.at[i, k], device_id=(px, py, c), device_id_type=MESH)
                cp.start()
                sent.append(cp)
        passed = []
        for i in range(n + nw):
            rows = ins[i].shape[0]
            for k, (px, py) in enumerate(peers):
                landed = _shard_half(outs[i], 2 * px + py, c, rows) if i < n else outs[i].at[2 * px + py]
                pltpu.make_async_remote_copy(src_ref=landed, dst_ref=landed, send_sem=send_sems.at[i, k],
                                             recv_sem=recv_sems.at[i, k], device_id=(px, py, c), device_id_type=MESH).wait_recv()
                if i < n:
                    cp = pltpu.make_async_remote_copy(src_ref=landed, dst_ref=landed, send_sem=pass_send.at[i, k],
                                                      recv_sem=pass_recv.at[i, k], device_id=(x, y, 1 - c), device_id_type=MESH)
                    cp.start()
                    passed.append(cp)
        for cp in sent:
            cp.wait_send()
        for cp in passed:
            cp.wait()
        for cp in local:
            cp.wait()

    return pl.pallas_call(
        body, name="weight_allgather",
        in_specs=[ANY] * (n + nw), out_specs=[ANY] * (n + nw), out_shape=outs_shape,
        scratch_shapes=[pltpu.SemaphoreType.DMA((n + nw, 3)), pltpu.SemaphoreType.DMA((n + nw, 3)),
                        pltpu.SemaphoreType.DMA((n, 3)), pltpu.SemaphoreType.DMA((n, 3)),
                        pltpu.SemaphoreType.DMA((n + nw,)), pltpu.SemaphoreType.DMA((n + nw,))],
    )(*shards, *whole)


HBM_SPEC = pl.BlockSpec(memory_space=pltpu.HBM)
SEM_SPEC = pl.BlockSpec(memory_space=pltpu.SEMAPHORE)
DATAFLOW = pltpu.SideEffectType.DATAFLOW_SIDE_EFFECTING


def _in_hbm(a):
    return pltpu.with_memory_space_constraint(a, pltpu.HBM)


def _split_start(name, groups, after):
    spans, arrays = [], []
    for srcs, lands, _, _ in groups:
        spans.append((len(arrays), len(srcs), len(lands)))
        arrays += list(srcs) + list(lands)
    na, ng = len(arrays), len(groups)

    def body(*refs):
        sems, token = refs[na + 1:na + 1 + 2 * ng], refs[-1]
        for g, (_, _, _, copies) in enumerate(groups):
            off, ns, nl = spans[g]
            for src, dst, dev, idx in copies(refs[off:off + ns], refs[off + ns:off + ns + nl]):
                pltpu.make_async_remote_copy(src_ref=src, dst_ref=dst, send_sem=sems[2 * g].at[idx], recv_sem=sems[2 * g + 1].at[idx],
                                             device_id=dev, device_id_type=MESH).start()
        token[...] = jnp.zeros_like(token)

    outs = pl.pallas_call(
        body, name=name,
        in_specs=[HBM_SPEC] * na + [ANY],
        out_specs=[SEM_SPEC] * (2 * ng) + [HBM_SPEC] * na + [pl.BlockSpec(memory_space=pltpu.VMEM)],
        out_shape=[pltpu.SemaphoreType.DMA((n_sems,)) for _, _, n_sems, _ in groups for _ in range(2)]
        + [pltpu.HBM(a.shape, a.dtype) for a in arrays] + [jax.ShapeDtypeStruct((SUBLANES, LANES), F32)],
        input_output_aliases={i: 2 * ng + i for i in range(na)},
        compiler_params=pltpu.CompilerParams(has_side_effects=DATAFLOW),
    )(*[_in_hbm(a) for a in arrays], after)
    started = []
    for g, (off, ns, nl) in enumerate(spans):
        thru = outs[2 * ng + off:2 * ng + off + ns + nl]
        started.append(dict(send=outs[2 * g], recv=outs[2 * g + 1], srcs=list(thru[:ns]), lands=list(thru[ns:]),
                            tile=outs[-1], token=outs[-1][0, 0]))
    return started


def _split_wait(name, started, copies, after):
    n, m = len(started["srcs"]), len(started["lands"])

    def body(*refs):
        src_refs, land_refs = refs[:n], refs[n:n + m]
        send_sem, recv_sem = refs[n + m], refs[n + m + 1]
        for src, dst, dev, idx in copies(src_refs, land_refs):
            cp = pltpu.make_async_remote_copy(src_ref=src, dst_ref=dst, send_sem=send_sem.at[idx], recv_sem=recv_sem.at[idx],
                                              device_id=dev, device_id_type=MESH)
            cp.wait_send()
            cp.wait_recv()

    arrays = started["srcs"] + started["lands"]
    outs = pl.pallas_call(
        body, name=name,
        in_specs=[HBM_SPEC] * (n + m) + [SEM_SPEC, SEM_SPEC, ANY],
        out_specs=[HBM_SPEC] * (n + m),
        out_shape=[pltpu.HBM(a.shape, a.dtype) for a in arrays],
        input_output_aliases={i: i for i in range(n + m)},
        compiler_params=pltpu.CompilerParams(has_side_effects=DATAFLOW),
    )(*arrays, started["send"], started["recv"], after)
    return list(outs)


def _gather_copies(srcs, lands):
    x, y, c = _position()
    chip = 2 * x + y
    targets = [(px, py, c) for px, py in _chip_peers(x, y)] + [(x, y, 1 - c)]
    return [(s, l.at[chip], dev, len(targets) * i + k) for i, (s, l) in enumerate(zip(srcs, lands)) for k, dev in enumerate(targets)]


def _sibling_copies(srcs, lands):
    x, y, c = _position()
    return [(_half_rows(srcs[0], 1 - c, srcs[0].shape[1]), lands[0], (x, y, 1 - c), 0)]


def _exchange_copies(srcs, lands):
    x, y, c = _position()
    return [(srcs[0].at[2 * px + py], lands[0].at[k], (px, py, c), k) for k, (px, py) in enumerate(_chip_peers(x, y))]


def _pair_sum_call(grad, recv, core, name):
    _, h, B = recv.shape
    tr = _row_tile(h, B)

    def body(core_ref, g_ref, r_ref, o_ref, ob_ref):
        s = g_ref[...] + r_ref[...]
        o_ref[...] = s
        ob_ref[...] = s.astype(BF16)

    g_spec = pl.BlockSpec((None, tr, B), lambda q, i, core_ref: (q, core_ref[0] * (h // tr) + i, 0))
    spec = pl.BlockSpec((None, tr, B), lambda q, i, core_ref: (q, i, 0))
    return pl.pallas_call(
        body, name=name,
        grid_spec=pltpu.PrefetchScalarGridSpec(num_scalar_prefetch=1, grid=(N_CHIPS, h // tr), in_specs=[g_spec, spec],
                                               out_specs=[spec, spec]),
        out_shape=[jax.ShapeDtypeStruct(recv.shape, F32), jax.ShapeDtypeStruct(recv.shape, BF16)],
        compiler_params=_params(("parallel", "parallel")),
    )(core, grad, recv)


def _chip_sum_call(partial, recv, chip_core, name):
    _, h, B = recv.shape
    tr = _row_tile(h, B)

    def body(cc_ref, p_ref, r_ref, o_ref):
        o_ref[...] = ((p_ref[...] + r_ref[0].astype(F32)) + r_ref[1].astype(F32)) + r_ref[2].astype(F32)

    return pl.pallas_call(
        body, name=name,
        grid_spec=pltpu.PrefetchScalarGridSpec(
            num_scalar_prefetch=1, grid=(h // tr,),
            in_specs=[pl.BlockSpec((None, tr, B), lambda i, cc_ref: (cc_ref[0], i, 0)),
                      pl.BlockSpec((3, tr, B), lambda i, cc_ref: (0, i, 0))],
            out_specs=pl.BlockSpec((tr, B), lambda i, cc_ref: (cc_ref[1] * (h // tr) + i, 0))),
        out_shape=jax.ShapeDtypeStruct((2 * h, B), F32),
        compiler_params=_params(("parallel",)),
    )(chip_core, partial, recv)


def _sibling_assemble_call(shards, name="grad_sibling_assemble"):
    n = len(shards)

    def body(*refs):
        ins, outs = refs[:n], refs[n:2 * n]
        send_sems, recv_sems = refs[2 * n:]
        x, y, c = _position()
        copies = []
        for i in range(n):
            rows = shards[i].shape[0]
            cp = pltpu.make_async_remote_copy(src_ref=_half_rows(ins[i], c, rows), dst_ref=_half_rows(outs[i], c, rows),
                                              send_sem=send_sems.at[i], recv_sem=recv_sems.at[i],
                                              device_id=(x, y, 1 - c), device_id_type=MESH)
            cp.start()
            copies.append(cp)
        for cp in copies:
            cp.wait()

    return pl.pallas_call(
        body, name=name,
        in_specs=[ANY] * n, out_specs=[ANY] * n,
        out_shape=[jax.ShapeDtypeStruct(s.shape, F32) for s in shards],
        input_output_aliases={i: i for i in range(n)},
        scratch_shapes=[pltpu.SemaphoreType.DMA((n,)), pltpu.SemaphoreType.DMA((n,))],
    )(*shards)


N_DEVICES = 8


def _allsum_copies(srcs, lands):
    x, y, c = _position()
    me = 4 * x + 2 * y + c
    out = []
    for k in range(1, N_DEVICES):
        peer = (1 - x if k & 4 else x, 1 - y if k & 2 else y, 1 - c if k & 1 else c)
        out.append((srcs[0], lands[0].at[me], peer, k - 1))
    return out


def _ordered_sum_call(mine, landed, me):
    rows = mine.shape[0]

    def body(me_ref, x_ref, l_ref, o_ref):
        acc = jnp.where(me_ref[0] == 0, x_ref[...], l_ref[0])
        for d in range(1, N_DEVICES):
            acc = acc + jnp.where(me_ref[0] == d, x_ref[...], l_ref[d])
        o_ref[...] = acc

    return pl.pallas_call(
        body, name="small_grad_sum",
        in_specs=[pl.BlockSpec(memory_space=pltpu.SMEM), pl.BlockSpec(memory_space=pltpu.VMEM), pl.BlockSpec(memory_space=pltpu.VMEM)],
        out_specs=pl.BlockSpec(memory_space=pltpu.VMEM),
        out_shape=jax.ShapeDtypeStruct((rows, LANES), F32),
    )(me, mine, landed)


def _pack(arrays):
    flat = jnp.concatenate([a.reshape(-1).astype(F32) for a in arrays])
    rows = -(-flat.shape[0] // LANES)
    rows = -(-rows // SUBLANES) * SUBLANES
    flat = jnp.pad(flat, (0, rows * LANES - flat.shape[0]))
    return flat.reshape(rows, LANES)


def _unpack(packed, shapes):
    flat = packed.reshape(-1)
    out, off = [], 0
    for shp in shapes:
        size = int(np.prod(shp))
        out.append(flat[off:off + size].reshape(shp))
        off += size
    return out


def _local_step(xs, target, P, late_weights, on_grad):
    S, D = xs.shape
    qkv_width = 3 * N_HEADS * HEAD_DIM
    glu_col0, gate_col0 = qkv_width, qkv_width + 2 * D
    shard_major = lambda g: g.reshape(N_CHIPS, g.shape[0] // N_CHIPS, g.shape[1])

    h1 = _rms_fwd_call(xs, P["norm_mix_pre"])
    proj = _matmul(h1, P["w_in"], "nn", "proj_in")
    buckets = _bucket_tables()
    bias = _bias_table_call(P["rel_bias"], buckets)
    parts = []
    for g in range(N_GROUPS):
        parts += _attn_fwd_call(proj, bias, g)
    a, a_bf, lse = _attn_merge_call(parts)
    P = dict(P, **late_weights("mix", a_bf))
    y_a = _matmul(a_bf, P["w_attn_out"], "nn", "attn_out")
    c1 = _conv_fwd_call(proj, glu_col0, P["conv_dw_w"], P["conv_dw_b"])
    cact = _ln_silu_call(c1, P["conv_ln_g"], P["conv_ln_b"])
    y_c = _matmul(cact, P["conv_pw_w"], "nn", "conv_pw")
    mixed = _mix_call(proj, gate_col0, P["b_gate"], y_a, y_c)
    out = _matmul(mixed, P["w_out"], "nn", "mix_out")
    x1, h2 = _res1_call(xs, out, P["norm_mix_post"], P["norm_ffn_pre"])
    P = dict(P, **late_weights("ffn", h2))
    u = _matmul(h2, P["w_up"], "nn", "ffn_up")
    f = _ffn_fwd_call(u, P["ffn_conv_w"], P["ffn_conv_b"])
    yff = _matmul(f, P["w_down"], "nn", "ffn_down")
    loss_tile, dx2, dyff, dg_ffn_post = _loss_call(yff, x1, P["norm_ffn_post"], target)

    G = {}
    G["norm_ffn_post"] = dg_ffn_post
    zero = on_grad("w_down", shard_major(_matmul(f, dyff, "tn", "ffn_down_dw")))
    df = _matmul(dyff, P["w_down"], "nt", "ffn_down_dx")
    du, dwg, dwv, dbg, dbv = _ffn_bwd_call(u, P["ffn_conv_w"], P["ffn_conv_b"] + zero, df)
    G["ffn_conv_w"] = jnp.concatenate([dwg[:FFN_CONV_WIDTH], dwv[:FFN_CONV_WIDTH]], axis=1)
    G["ffn_conv_b"] = jnp.concatenate([dbg, dbv], axis=1)
    zero = on_grad("w_up", _matmul(h2, du, "tn", "ffn_up_dw", out_shards=True))
    dh2 = _matmul(du, P["w_up"], "nt", "ffn_up_dx")
    dx1, dout, G["norm_ffn_pre"], G["norm_mix_post"] = _mid_bwd_call(x1, P["norm_ffn_pre"] + zero, dh2, dx2, out, P["norm_mix_post"])
    zero = on_grad("w_out", shard_major(_matmul(mixed, dout, "tn", "mix_out_dw")))
    dmixed = _matmul(dout, P["w_out"], "nt", "mix_out_dx")
    dya, dyc, dga, dgc, dba, dbc = _mix_bwd_call(dmixed, proj, gate_col0, P["b_gate"] + zero, y_a, y_c)
    G["b_gate"] = jnp.concatenate([dba, dbc], axis=1)
    zero = on_grad("w_attn_out", _matmul(a_bf, dya, "tn", "attn_out_dw", out_shards=True))
    zero = zero + on_grad("conv_pw_w", shard_major(_matmul(cact, dyc, "tn", "conv_pw_dw")))
    da = _matmul(dya, P["w_attn_out"], "nt", "attn_out_dx")
    dcact = _matmul(dyc, P["conv_pw_w"], "nt", "conv_pw_dx")
    dc1, G["conv_ln_g"], G["conv_ln_b"] = _ln_silu_bwd_call(c1, P["conv_ln_g"] + zero, P["conv_ln_b"], dcact)
    dval, dgate, dw_dw, G["conv_dw_b"] = _conv_bwd_call(proj, glu_col0, P["conv_dw_w"], dc1)
    G["conv_dw_w"] = dw_dw[:CONV_WIDTH]
    delta = _attn_delta_call(a, da)
    dqs, dks, dvs, dbs = [], [], [], []
    for g in range(N_GROUPS):
        dq, dk, dv, db = _attn_bwd_call(proj, bias, da, lse, delta, g)
        dqs.append(dq)
        dks.append(dk)
        dvs.append(dv)
        dbs.append(db)
    G["rel_bias"] = _bias_grad_call(jnp.concatenate(dbs, axis=0), buckets)
    dproj = _dproj_call(dqs + dks + dvs, [dval, dgate, dga, dgc])
    zero = on_grad("w_in", _matmul(h1, dproj, "tn", "proj_in_dw", out_shards=True))
    dh1 = _matmul(dproj, P["w_in"], "nt", "proj_in_dx")
    zero = zero + on_grad(None, dh1)
    grad_x, G["norm_mix_pre"] = _in_bwd_call(xs, P["norm_mix_pre"] + zero, dh1, dx1)
    return loss_tile, grad_x, G


def kernel(x, w_in, b_gate, rel_bias, w_attn_out, conv_dw_w, conv_dw_b, conv_ln_g, conv_ln_b, conv_pw_w, w_out, norm_mix_pre, norm_mix_post, norm_ffn_pre, norm_ffn_post, w_up, ffn_conv_w, ffn_conv_b, w_down, loss_target, m_w_in, m_b_gate, m_rel_bias, m_w_attn_out, m_conv_dw_w, m_conv_dw_b, m_conv_ln_g, m_conv_ln_b, m_conv_pw_w, m_w_out, m_norm_mix_pre, m_norm_mix_post, m_norm_ffn_pre, m_norm_ffn_post, m_w_up, m_ffn_conv_w, m_ffn_conv_b, m_w_down, v_w_in, v_b_gate, v_rel_bias, v_w_attn_out, v_conv_dw_w, v_conv_dw_b, v_conv_ln_g, v_conv_ln_b, v_conv_pw_w, v_w_out, v_norm_mix_pre, v_norm_mix_post, v_norm_ffn_pre, v_norm_ffn_post, v_w_up, v_ffn_conv_w, v_ffn_conv_b, v_w_down):
    weights = dict(w_in=w_in, b_gate=b_gate, rel_bias=rel_bias, w_attn_out=w_attn_out, conv_dw_w=conv_dw_w, conv_dw_b=conv_dw_b,
                   conv_ln_g=conv_ln_g, conv_ln_b=conv_ln_b, conv_pw_w=conv_pw_w, w_out=w_out, norm_mix_pre=norm_mix_pre,
                   norm_mix_post=norm_mix_post, norm_ffn_pre=norm_ffn_pre, norm_ffn_post=norm_ffn_post, w_up=w_up,
                   ffn_conv_w=ffn_conv_w, ffn_conv_b=ffn_conv_b, w_down=w_down)
    m_in = dict(w_in=m_w_in, b_gate=m_b_gate, rel_bias=m_rel_bias, w_attn_out=m_w_attn_out, conv_dw_w=m_conv_dw_w,
                conv_dw_b=m_conv_dw_b, conv_ln_g=m_conv_ln_g, conv_ln_b=m_conv_ln_b, conv_pw_w=m_conv_pw_w, w_out=m_w_out,
                norm_mix_pre=m_norm_mix_pre, norm_mix_post=m_norm_mix_post, norm_ffn_pre=m_norm_ffn_pre,
                norm_ffn_post=m_norm_ffn_post, w_up=m_w_up, ffn_conv_w=m_ffn_conv_w, ffn_conv_b=m_ffn_conv_b, w_down=m_w_down)
    v_in = dict(w_in=v_w_in, b_gate=v_b_gate, rel_bias=v_rel_bias, w_attn_out=v_w_attn_out, conv_dw_w=v_conv_dw_w,
                conv_dw_b=v_conv_dw_b, conv_ln_g=v_conv_ln_g, conv_ln_b=v_conv_ln_b, conv_pw_w=v_conv_pw_w, w_out=v_w_out,
                norm_mix_pre=v_norm_mix_pre, norm_mix_post=v_norm_mix_post, norm_ffn_pre=v_norm_ffn_pre,
                norm_ffn_post=v_norm_ffn_post, w_up=v_w_up, ffn_conv_w=v_ffn_conv_w, ffn_conv_b=v_ffn_conv_b, w_down=v_w_down)
    names = list(weights)
    xi, yi, ci = _position()
    chip = 2 * xi + yi
    core_arr = jnp.reshape(ci, (1,)).astype(jnp.int32)

    xs = x[0]
    target = loss_target[0]
    S, D = xs.shape

    big = ["w_in", "w_attn_out", "conv_pw_w", "w_out", "w_up", "w_down"]
    row_sharded = ("conv_pw_w", "w_out", "w_down")
    bf16_shard = {k: weights[k][0].astype(BF16) for k in big}
    natural = lambda k, g: g.reshape(-1, g.shape[2]) if k in row_sharded else g
    w_in_full, dw4, fc4 = _allgather_call([bf16_shard["w_in"]], [conv_dw_w[0], ffn_conv_w[0]])
    late_sets = dict(mix=["w_attn_out", "conv_pw_w", "w_out"], ffn=["w_up", "w_down"])
    late_groups = []
    for keys in late_sets.values():
        srcs = [bf16_shard[k] for k in keys]
        late_groups.append((srcs, [lax.empty((N_CHIPS,) + s.shape, BF16) for s in srcs], 4 * len(keys), _gather_copies))
    started = dict(zip(late_sets, _split_start("gather_late_start", late_groups, w_in_full)))
    launched = started["mix"]["token"]

    def late_weights(tag, after):
        landed = _split_wait(f"gather_{tag}_wait", started[tag], _gather_copies, after)[len(late_sets[tag]):]
        return {k: natural(k, g) for k, g in zip(late_sets[tag], landed)}

    chip_core = jnp.stack([chip, ci]).astype(jnp.int32)
    exchanging, pending = {}, {}

    def launch(tag, g3, after):
        keys, groups, partial = [], [], {}
        for k in list(exchanging):
            gk, r1 = _split_wait(f"sibling_exchange_wait_{k}", exchanging.pop(k), _sibling_copies, after)
            partial[k], s16 = _pair_sum_call(gk, r1, core_arr, f"pair_sum_{k}")
            keys.append(k)
            groups.append(([s16], [lax.empty((3,) + s16.shape[1:], BF16)], 3, _exchange_copies))
        if g3 is not None:
            groups.append(([g3], [lax.empty((N_CHIPS, g3.shape[1] // 2, g3.shape[2]), F32)], 1, _sibling_copies))
        begun = _split_start(f"grad_exchange_start_{tag}", groups, core_arr)
        for k, st in zip(keys, begun):
            pending[k] = (partial[k], st)
        if g3 is not None:
            exchanging[tag] = begun[-1]
        return begun[0]["token"]

    def on_grad(k, g3):
        if k is None:
            return launch("last", None, g3[:SUBLANES, :LANES])
        return launch(k, g3, g3[0, :SUBLANES, :LANES])

    def finish(keys, after, tag):
        halves = []
        for k in keys:
            s32, st = pending[k]
            recv2 = _split_wait(f"chip_exchange_wait_{k}", st, _exchange_copies, after)[1]
            halves.append(_chip_sum_call(s32, recv2, chip_core, f"chip_sum_{k}"))
        return dict(zip(keys, _sibling_assemble_call(halves, f"grad_sibling_assemble_{tag}")))

    P = dict(w_in=w_in_full, conv_dw_w=jnp.concatenate(list(dw4), axis=1), ffn_conv_w=jnp.concatenate(list(fc4), axis=1),
             b_gate=b_gate, rel_bias=rel_bias, conv_dw_b=conv_dw_b, conv_ln_g=conv_ln_g, conv_ln_b=conv_ln_b,
             norm_mix_pre=norm_mix_pre + launched, norm_mix_post=norm_mix_post, norm_ffn_pre=norm_ffn_pre,
             norm_ffn_post=norm_ffn_post, ffn_conv_b=ffn_conv_b)
    loss_tile, grad_x, G = _local_step(xs, target, P, late_weights, on_grad)

    small = [k for k in names if k not in big]
    packed = _pack([loss_tile[:1]] + [G[k] for k in small])
    (allsum,) = _split_start("small_grad_allsum_start",
                             [([packed], [jnp.zeros((N_DEVICES,) + packed.shape, F32)], N_DEVICES - 1, _allsum_copies)], core_arr)

    reduced, grads, deltas, new_m, new_v = {}, {}, {}, {}, {}

    def update(keys):
        for k in keys:
            d, mn, vn = _adamw_call(weights[k][0], reduced[k], m_in[k][0], v_in[k][0], f"adamw_{k}")
            grads[k], deltas[k], new_m[k], new_v[k] = reduced[k][None], d[None], mn[None], vn[None]

    others = [k for k in big if k != "w_in"]
    reduced.update(finish(others, allsum["tile"], "others"))
    update(others)
    reduced.update(finish(["w_in"], deltas["w_up"], "w_in"))
    update(["w_in"])

    me = jnp.reshape(4 * xi + 2 * yi + ci, (1,)).astype(jnp.int32)
    mine, landed = _split_wait("small_grad_allsum_wait", allsum, _allsum_copies, deltas["w_in"])
    summed_block = _ordered_sum_call(mine, landed, me)
    loss_row, *summed = _unpack(summed_block, [(1, LANES)] + [G[k].shape for k in small])
    loss = loss_row[0, 0]
    for k, gsum in zip(small, summed):
        if k in ("conv_dw_w", "ffn_conv_w"):
            cols = weights[k].shape[2]
            reduced[k] = lax.dynamic_slice_in_dim(gsum, chip * cols, cols, axis=1)
        else:
            reduced[k] = gsum
    flat2 = lambda t: t.reshape(-1, t.shape[-1]) if t.ndim == 3 else t
    ds, mns, vns = _adamw_small_call([flat2(weights[k]) for k in small], [reduced[k] for k in small],
                                     [flat2(m_in[k]) for k in small], [flat2(v_in[k]) for k in small])
    for k, dk_, mk, vk in zip(small, ds, mns, vns):
        shape = weights[k].shape
        grads[k], deltas[k], new_m[k], new_v[k] = reduced[k].reshape(shape), dk_.reshape(shape), mk.reshape(shape), vk.reshape(shape)

    return (loss, grad_x[None], *[grads[k] for k in names], *[deltas[k] for k in names],
            *[new_m[k] for k in names], *[new_v[k] for k in names])
```

```python
import functools
import math

import jax
import jax.numpy as jnp
import numpy as np
from jax import lax
from jax.experimental import pallas as pl
from jax.experimental.pallas import tpu as pltpu

F32 = jnp.float32
BF16 = jnp.bfloat16
MESH = pl.DeviceIdType.MESH

HEAD_DIM = 128
HEADS_PER_GROUP = 4
DILATED_PATTERNS = ((128, 1), (512, 4), (2048, 16))
N_GROUPS = 3
N_HEADS = N_GROUPS * HEADS_PER_GROUP
SPAN = 128
GROUP_WIDTH = HEADS_PER_GROUP * HEAD_DIM
CONV_WIDTH = 31
FFN_CONV_WIDTH = 3
N_BUCKETS = 32
MAX_DISTANCE = 2048
RMS_EPS = 1e-6
LN_EPS = 1e-5
NEG_INF = -1e30
ADAM_LR = 0.001
ADAM_B1 = 0.9
ADAM_B2 = 0.999
ADAM_EPS = 1e-08
ADAM_WD = 0.01
ADAM_STEP = 10

LANES = 128
SUBLANES = 8
ROW_TILE = 512
GATE_ROWS, GATE_COLS = 512, 512
TIME_BLOCK = 128
CONV_PAD = 32
FFN_PAD = 8
VMEM_LIMIT = 56 << 20


def _params(sem=None, vmem=None):
    kw = {}
    if sem is not None:
        kw["dimension_semantics"] = sem
    if vmem is not None:
        kw["vmem_limit_bytes"] = vmem
    return pltpu.CompilerParams(**kw)


def _pick(n, cands):
    for c in cands:
        if n % c == 0:
            return c
    return n


ELEMENTWISE_TILE_BYTES = 3 << 19


def _row_tile(rows, cols):
    for align in (16, SUBLANES):
        fits = [t for t in range(align, rows + 1, align) if rows % t == 0 and t * cols * 4 <= ELEMENTWISE_TILE_BYTES]
        if fits:
            return max(fits)
    return SUBLANES


N_CHIPS = 4
M_TILES = (1024, 1408, 512, 256, 128)
N_TILES = (1024, 512, 1408, 256, 128)
K_TILES = (2176, 2048, 1408, 1024, 512, 256, 128)


def _matmul(a, b, mode, name, out_shards=False, tm=None):
    assert a.dtype == BF16 and b.dtype == BF16, (name, a.dtype, b.dtype)
    b3 = b.ndim == 3
    tn = tk = None
    halves = None
    if mode == "nn":
        M, K = a.shape
        N = b.shape[-1] * (N_CHIPS if b3 else 1)
        tn = b.shape[-1] if b3 else None
    elif mode == "nt":
        if a.ndim == 3:
            halves = a.shape[2]
        M, K = a.shape[-2], a.shape[-1] * (a.shape[0] if a.ndim == 3 else 1)
        N = b.shape[-2]
        tk = b.shape[-1] if b3 else None
    else:
        if b3:
            halves = b.shape[2]
        K, M = a.shape
        N = b.shape[-1] * (b.shape[0] if b3 else 1)
        tn = N // N_CHIPS if out_shards else None
    tm = tm or _pick(M, M_TILES)
    tn = tn or _pick(N, N_TILES)
    tk = tk or _pick(K, K_TILES)
    nk = K // tk
    dn = {"nn": (((1,), (0,)), ((), ())), "nt": (((1,), (1,)), ((), ())), "tn": (((0,), (0,)), ((), ()))}[mode]

    def body(a_ref, b_ref, o_ref):
        if nk == 1:
            o_ref[...] = lax.dot_general(a_ref[...], b_ref[...], dn, preferred_element_type=F32)
        else:
            @pl.when(pl.program_id(2) == 0)
            def _():
                o_ref[...] = jnp.zeros_like(o_ref)

            o_ref[...] += lax.dot_general(a_ref[...], b_ref[...], dn, preferred_element_type=F32)

    if mode == "tn":
        a_spec = pl.BlockSpec((tk, tm), lambda i, j, k: (k, i))
    elif halves:
        per = halves // tk
        a_spec = pl.BlockSpec((None, tm, tk), lambda i, j, k: (k // per, i, k % per))
    else:
        a_spec = pl.BlockSpec((tm, tk), lambda i, j, k: (i, k))
    if mode == "nn":
        b_spec = pl.BlockSpec((None, tk, tn), lambda i, j, k: (j, k, 0)) if b3 else pl.BlockSpec((tk, tn), lambda i, j, k: (k, j))
    elif mode == "nt":
        b_spec = pl.BlockSpec((None, tn, tk), lambda i, j, k: (k, j, 0)) if b3 else pl.BlockSpec((tn, tk), lambda i, j, k: (j, k))
    elif halves:
        per = halves // tn
        b_spec = pl.BlockSpec((None, tk, tn), lambda i, j, k: (j // per, k, j % per))
    else:
        b_spec = pl.BlockSpec((tk, tn), lambda i, j, k: (k, j))
    if out_shards:
        out_spec = pl.BlockSpec((None, tm, tn), lambda i, j, k: (j, i, 0))
        out_shape = jax.ShapeDtypeStruct((N_CHIPS, M, tn), F32)
    else:
        out_spec = pl.BlockSpec((tm, tn), lambda i, j, k: (i, j))
        out_shape = jax.ShapeDtypeStruct((M, N), F32)
    return pl.pallas_call(
        body, name=name, grid=(M // tm, N // tn, nk),
        in_specs=[a_spec, b_spec], out_specs=out_spec, out_shape=out_shape,
        compiler_params=_params(("parallel", "parallel", "arbitrary"), VMEM_LIMIT),
    )(a, b)


def _rms(x, g):
    r = lax.rsqrt(jnp.mean(x * x, axis=-1, keepdims=True) + RMS_EPS)
    return x * r * g


def _rms_bwd(x, g, dy):
    r = lax.rsqrt(jnp.mean(x * x, axis=-1, keepdims=True) + RMS_EPS)
    n = x * r
    dn = dy * g
    dx = r * (dn - n * jnp.mean(dn * n, axis=-1, keepdims=True))
    return dx, jnp.sum(dy * n, axis=0, keepdims=True)


def _sigmoid(x):
    return 1.0 / (1.0 + jnp.exp(-x))


_GELU_C = math.sqrt(2.0 / math.pi)


def _gelu(x):
    return 0.5 * x * (1.0 + jnp.tanh(_GELU_C * (x + 0.044715 * x * x * x)))


def _gelu_and_grad(x):
    x2 = x * x
    t = jnp.tanh(_GELU_C * x * (1.0 + 0.044715 * x2))
    half = 0.5 * (1.0 + t)
    return x * half, half + (0.5 * _GELU_C) * x * (1.0 - t * t) * (1.0 + (3.0 * 0.044715) * x2)


def _row_spec(width, col_block=0):
    return pl.BlockSpec((ROW_TILE, width), lambda i: (i, col_block))


def _vec_spec(width, col_block=0):
    return pl.BlockSpec((1, width), lambda i: (0, col_block))


def _accumulate(ref, part):
    @pl.when(pl.program_id(0) == 0)
    def _():
        ref[...] = part

    @pl.when(pl.program_id(0) > 0)
    def _():
        ref[...] += part


def _rms_fwd_call(x, g):
    S, D = x.shape

    def body(x_ref, g_ref, h_ref):
        h_ref[...] = _rms(x_ref[...], g_ref[...]).astype(BF16)

    return pl.pallas_call(
        body, name="rms_mix_pre", grid=(S // ROW_TILE,),
        in_specs=[_row_spec(D), _vec_spec(D)], out_specs=_row_spec(D),
        out_shape=jax.ShapeDtypeStruct((S, D), BF16),
        compiler_params=_params(("parallel",)),
    )(x, g)


def _ln_silu_call(c1, g, b):
    S, C = c1.shape

    def body(c_ref, g_ref, b_ref, o_ref):
        xv = c_ref[...]
        mu = jnp.mean(xv, axis=-1, keepdims=True)
        xc = xv - mu
        var = jnp.mean(xc * xc, axis=-1, keepdims=True)
        z = xc * lax.rsqrt(var + LN_EPS) * g_ref[...] + b_ref[...]
        o_ref[...] = (z * _sigmoid(z)).astype(BF16)

    return pl.pallas_call(
        body, name="conv_ln_silu", grid=(S // ROW_TILE,),
        in_specs=[_row_spec(C), _vec_spec(C), _vec_spec(C)], out_specs=_row_spec(C),
        out_shape=jax.ShapeDtypeStruct((S, C), BF16),
        compiler_params=_params(("parallel",)),
    )(c1, g, b)


def _ln_silu_bwd_call(c1, g, b, dc):
    S, C = c1.shape

    def body(c_ref, g_ref, b_ref, dc_ref, dx_ref, dg_ref, db_ref):
        xv = c_ref[...]
        mu = jnp.mean(xv, axis=-1, keepdims=True)
        xc = xv - mu
        rs = lax.rsqrt(jnp.mean(xc * xc, axis=-1, keepdims=True) + LN_EPS)
        xh = xc * rs
        z = xh * g_ref[...] + b_ref[...]
        sg = _sigmoid(z)
        dz = dc_ref[...] * (sg * (1.0 + z * (1.0 - sg)))
        dxh = dz * g_ref[...]
        dx_ref[...] = rs * (dxh - jnp.mean(dxh, axis=-1, keepdims=True) - xh * jnp.mean(dxh * xh, axis=-1, keepdims=True))
        _accumulate(dg_ref, jnp.sum(dz * xh, axis=0, keepdims=True))
        _accumulate(db_ref, jnp.sum(dz, axis=0, keepdims=True))

    return pl.pallas_call(
        body, name="conv_ln_silu_bwd", grid=(S // ROW_TILE,),
        in_specs=[_row_spec(C), _vec_spec(C), _vec_spec(C), _row_spec(C)],
        out_specs=[_row_spec(C), _vec_spec(C), _vec_spec(C)],
        out_shape=[jax.ShapeDtypeStruct((S, C), F32), jax.ShapeDtypeStruct((1, C), F32), jax.ShapeDtypeStruct((1, C), F32)],
        compiler_params=_params(("arbitrary",)),
    )(c1, g, b, dc)


def _mix_call(proj, gate_col0, b_gate, y_a, y_c):
    S, D = y_a.shape
    w = GATE_COLS
    nc = D // w
    ga0, gc0 = gate_col0 // w, (gate_col0 + D) // w

    def body(ga_ref, gc_ref, ba_ref, bc_ref, ya_ref, yc_ref, o_ref):
        o_ref[...] = (_sigmoid(ga_ref[...] + ba_ref[...]) * ya_ref[...]
                      + _sigmoid(gc_ref[...] + bc_ref[...]) * yc_ref[...]).astype(BF16)

    tile = lambda off: pl.BlockSpec((GATE_ROWS, w), lambda i, j: (i, off + j))
    vec = lambda off: pl.BlockSpec((1, w), lambda i, j: (0, off + j))
    return pl.pallas_call(
        body, name="gate_mix", grid=(S // GATE_ROWS, nc),
        in_specs=[tile(ga0), tile(gc0), vec(0), vec(nc), tile(0), tile(0)],
        out_specs=tile(0), out_shape=jax.ShapeDtypeStruct((S, D), BF16),
        compiler_params=_params(("parallel", "parallel")),
    )(proj, proj, b_gate, b_gate, y_a, y_c)


def _mix_bwd_call(dmixed, proj, gate_col0, b_gate, y_a, y_c):
    S, D = y_a.shape
    w = GATE_COLS
    nc = D // w
    ga0, gc0 = gate_col0 // w, (gate_col0 + D) // w

    def body(dm_ref, ga_ref, gc_ref, ba_ref, bc_ref, ya_ref, yc_ref, dya_ref, dyc_ref, dga_ref, dgc_ref, dba_ref, dbc_ref):
        dm = dm_ref[...]
        sa = _sigmoid(ga_ref[...] + ba_ref[...])
        sc = _sigmoid(gc_ref[...] + bc_ref[...])
        dya_ref[...] = (dm * sa).astype(BF16)
        dyc_ref[...] = (dm * sc).astype(BF16)
        dga = dm * ya_ref[...] * sa * (1.0 - sa)
        dgc = dm * yc_ref[...] * sc * (1.0 - sc)
        dga_ref[...] = dga.astype(BF16)
        dgc_ref[...] = dgc.astype(BF16)
        pa = jnp.sum(dga, axis=0, keepdims=True)
        pc = jnp.sum(dgc, axis=0, keepdims=True)

        @pl.when(pl.program_id(1) == 0)
        def _():
            dba_ref[...] = pa
            dbc_ref[...] = pc

        @pl.when(pl.program_id(1) > 0)
        def _():
            dba_ref[...] += pa
            dbc_ref[...] += pc

    tile = lambda off: pl.BlockSpec((GATE_ROWS, w), lambda j, i: (i, off + j))
    vec = lambda off: pl.BlockSpec((1, w), lambda j, i: (0, off + j))
    return pl.pallas_call(
        body, name="gate_mix_bwd", grid=(nc, S // GATE_ROWS),
        in_specs=[tile(0), tile(ga0), tile(gc0), vec(0), vec(nc), tile(0), tile(0)],
        out_specs=[tile(0), tile(0), tile(0), tile(0), vec(0), vec(0)],
        out_shape=[jax.ShapeDtypeStruct((S, D), BF16)] * 4 + [
                   jax.ShapeDtypeStruct((1, D), F32), jax.ShapeDtypeStruct((1, D), F32)],
        compiler_params=_params(("parallel", "arbitrary")),
    )(dmixed, proj, proj, b_gate, b_gate, y_a, y_c)


def _res1_call(x, out, g_post, g_pre):
    S, D = x.shape

    def body(x_ref, o_ref, gp_ref, gq_ref, x1_ref, h2_ref):
        x1 = x_ref[...] + _rms(o_ref[...], gp_ref[...])
        x1_ref[...] = x1
        h2_ref[...] = _rms(x1, gq_ref[...]).astype(BF16)

    return pl.pallas_call(
        body, name="residual_mix", grid=(S // ROW_TILE,),
        in_specs=[_row_spec(D), _row_spec(D), _vec_spec(D), _vec_spec(D)],
        out_specs=[_row_spec(D), _row_spec(D)],
        out_shape=[jax.ShapeDtypeStruct((S, D), F32), jax.ShapeDtypeStruct((S, D), BF16)],
        compiler_params=_params(("parallel",)),
    )(x, out, g_post, g_pre)


def _loss_call(y, x1, g_post, target):
    S, D = y.shape

    def body(y_ref, x1_ref, g_ref, t_ref, loss_ref, dx_ref, dy_ref, dg_ref):
        yv, gv = y_ref[...], g_ref[...]
        err = x1_ref[...] + _rms(yv, gv) - t_ref[...]
        dx2 = err * (1.0 / D)
        dx_ref[...] = dx2
        dy, dg = _rms_bwd(yv, gv, dx2)
        dy_ref[...] = dy.astype(BF16)
        _accumulate(dg_ref, dg)
        part = 0.5 * jnp.sum(jnp.mean(err * err, axis=-1, keepdims=True), axis=0, keepdims=True)
        _accumulate(loss_ref, jnp.broadcast_to(part, (SUBLANES, LANES)))

    return pl.pallas_call(
        body, name="residual_ffn_loss", grid=(S // ROW_TILE,),
        in_specs=[_row_spec(D), _row_spec(D), _vec_spec(D), _row_spec(D)],
        out_specs=[pl.BlockSpec((SUBLANES, LANES), lambda i: (0, 0)), _row_spec(D), _row_spec(D), _vec_spec(D)],
        out_shape=[jax.ShapeDtypeStruct((SUBLANES, LANES), F32), jax.ShapeDtypeStruct((S, D), F32),
                   jax.ShapeDtypeStruct((S, D), BF16), jax.ShapeDtypeStruct((1, D), F32)],
        compiler_params=_params(("arbitrary",)),
    )(y, x1, g_post, target)


def _mid_bwd_call(x1, g_pre, dh2, dx2, out, g_post):
    S, D = x1.shape

    def body(x1_ref, gq_ref, dh_ref, dx2_ref, o_ref, gp_ref, dx1_ref, do_ref, dgq_ref, dgp_ref):
        d, dgq = _rms_bwd(x1_ref[...], gq_ref[...], dh_ref[...])
        dx1 = dx2_ref[...] + d
        dx1_ref[...] = dx1
        do, dgp = _rms_bwd(o_ref[...], gp_ref[...], dx1)
        do_ref[...] = do.astype(BF16)
        _accumulate(dgq_ref, dgq)
        _accumulate(dgp_ref, dgp)

    return pl.pallas_call(
        body, name="residual_mix_bwd", grid=(S // ROW_TILE,),
        in_specs=[_row_spec(D), _vec_spec(D), _row_spec(D), _row_spec(D), _row_spec(D), _vec_spec(D)],
        out_specs=[_row_spec(D), _row_spec(D), _vec_spec(D), _vec_spec(D)],
        out_shape=[jax.ShapeDtypeStruct((S, D), F32), jax.ShapeDtypeStruct((S, D), BF16)] + [jax.ShapeDtypeStruct((1, D), F32)] * 2,
        compiler_params=_params(("arbitrary",)),
    )(x1, g_pre, dh2, dx2, out, g_post)


def _in_bwd_call(x, g, dh1, dx1):
    S, D = x.shape

    def body(x_ref, g_ref, dh_ref, dx1_ref, gx_ref, dg_ref):
        d, dg = _rms_bwd(x_ref[...], g_ref[...], dh_ref[...])
        gx_ref[...] = dx1_ref[...] + d
        _accumulate(dg_ref, dg)

    return pl.pallas_call(
        body, name="rms_mix_pre_bwd", grid=(S // ROW_TILE,),
        in_specs=[_row_spec(D), _vec_spec(D), _row_spec(D), _row_spec(D)],
        out_specs=[_row_spec(D), _vec_spec(D)],
        out_shape=[jax.ShapeDtypeStruct((S, D), F32), jax.ShapeDtypeStruct((1, D), F32)],
        compiler_params=_params(("arbitrary",)),
    )(x, g, dh1, dx1)


def _bucket_table(dilation):
    qi = np.arange(SPAN)[:, None]
    ki = np.arange(2 * SPAN)[None, :]
    dist = np.maximum(qi + SPAN - ki, 0) * dilation
    max_exact = N_BUCKETS // 2
    d = np.maximum(dist, 1).astype(np.float64)
    large = max_exact + (np.log(d / max_exact) / math.log(MAX_DISTANCE / max_exact) * (N_BUCKETS - max_exact)).astype(np.int32)
    large = np.minimum(large, N_BUCKETS - 1)
    return np.where(dist < max_exact, dist, large).astype(np.int32)


def _bucket_tables():
    return jnp.asarray(np.stack([_bucket_table(r) for _, r in DILATED_PATTERNS]))


def _bias_table_call(rel_bias, buckets):
    def body(rb_ref, bk_ref, o_ref):
        for h in range(N_HEADS):
            bk = bk_ref[h // HEADS_PER_GROUP]

            def step(b, acc):
                return jnp.where(bk == b, rb_ref[b, h], acc)

            o_ref[h] = lax.fori_loop(0, N_BUCKETS, step, jnp.zeros((SPAN, 2 * SPAN), F32))

    return pl.pallas_call(
        body, name="rel_bias_table",
        in_specs=[pl.BlockSpec(memory_space=pltpu.SMEM), pl.BlockSpec(memory_space=pltpu.VMEM)],
        out_specs=pl.BlockSpec(memory_space=pltpu.VMEM),
        out_shape=jax.ShapeDtypeStruct((N_HEADS, SPAN, 2 * SPAN), F32),
    )(rel_bias, buckets)


def _bias_grad_call(dbias, buckets):
    def body(db_ref, bk_ref, o_ref, rows_ref):
        for h in range(N_HEADS):
            bk = bk_ref[h // HEADS_PER_GROUP]
            dv = db_ref[h]

            def step(b, carry):
                rows_ref[h, b] = jnp.sum(jnp.where(bk == b, dv, 0.0), axis=0, keepdims=True)
                return carry

            lax.fori_loop(0, N_BUCKETS, step, 0)
        o_ref[...] = jnp.sum(rows_ref[...], axis=-1, keepdims=True)

    out = pl.pallas_call(
        body, name="rel_bias_grad",
        in_specs=[pl.BlockSpec(memory_space=pltpu.VMEM), pl.BlockSpec(memory_space=pltpu.VMEM)],
        out_specs=pl.BlockSpec(memory_space=pltpu.VMEM),
        out_shape=jax.ShapeDtypeStruct((N_HEADS, N_BUCKETS, 1, 1), F32),
        scratch_shapes=[pltpu.VMEM((N_HEADS, N_BUCKETS, 1, 2 * SPAN), F32)],
    )(dbias, buckets)
    return out.reshape(N_HEADS, N_BUCKETS).T


def _dot_nt(a, b):
    return lax.dot_general(a, b, (((1,), (1,)), ((), ())), preferred_element_type=F32)


def _dot_nn(a, b):
    return lax.dot_general(a, b, (((1,), (0,)), ((), ())), preferred_element_type=F32)


def _dot_tn(a, b):
    return lax.dot_general(a, b, (((0,), (0,)), ((), ())), preferred_element_type=F32)


def _band_masks(n, nb):
    qi = lax.broadcasted_iota(jnp.int32, (SPAN, SPAN), 0)
    ki = lax.broadcasted_iota(jnp.int32, (SPAN, SPAN), 1)
    prev_ok = jnp.logical_and(ki >= qi, n > 0)
    cur_ok = ki <= qi
    next_ok = jnp.logical_and(ki >= qi, n < nb - 1)
    return prev_ok, cur_ok, next_ok


def _wide_band_mask(n):
    qi = lax.broadcasted_iota(jnp.int32, (SPAN, 2 * SPAN), 0)
    ki = lax.broadcasted_iota(jnp.int32, (SPAN, 2 * SPAN), 1)
    prev_ok = jnp.logical_and(jnp.logical_and(ki < SPAN, ki >= qi), n > 0)
    cur_ok = jnp.logical_and(ki >= SPAN, ki - SPAN <= qi)
    return jnp.logical_or(prev_ok, cur_ok)


def _attn_plan(S, group):
    r = DILATED_PATTERNS[group][1]
    hp, per = (HEADS_PER_GROUP, 1) if r == 1 else (2, 4)
    return r, S // (r * SPAN), hp, per


def _residue_rows(rho, r):
    return slice(None) if r == 1 else pl.ds(rho, SPAN, stride=r)


def _for_residues(r, per, fn):
    if r == per:
        for u in range(per):
            fn(u)
        return

    def step(i, carry):
        for u in range(per):
            fn(i * per + u)
        return carry

    lax.fori_loop(0, r // per, step, 0)


def _attn_fwd_call(proj, bias, group):
    S = proj.shape[0]
    r, nb, hp, per = _attn_plan(S, group)
    scale = HEAD_DIM ** -0.5
    kinds = ("q", "kp", "kc", "vp", "vc") if nb > 1 else ("q", "kc", "vc")

    def body(*refs):
        ins = {kind: refs[i * hp:(i + 1) * hp] for i, kind in enumerate(kinds)}
        b_ref, o_ref, lse_ref = refs[len(kinds) * hp:]
        n = pl.program_id(1)
        prev_ok, cur_ok, _ = _band_masks(n, nb)

        band_ok = _wide_band_mask(n) if nb > 1 else cur_ok

        def residue(rho):
            rows = _residue_rows(rho, r)
            for j in range(hp):
                get = lambda kind: ins[kind][j][rows, :].astype(BF16)
                q = get("q")
                if nb > 1:
                    keys, vals, bias_j = jnp.concatenate([get("kp"), get("kc")], axis=0), jnp.concatenate([get("vp"), get("vc")], axis=0), b_ref[j]
                else:
                    keys, vals, bias_j = get("kc"), get("vc"), b_ref[j, :, SPAN:]
                s = jnp.where(band_ok, _dot_nt(q, keys) * scale + bias_j, NEG_INF)
                m = jnp.max(s, axis=-1, keepdims=True)
                p = jnp.exp(s - m)
                den = jnp.sum(p, axis=-1, keepdims=True)
                o_ref[j, rows, :] = _dot_nn(p.astype(BF16), vals) / den
                lse_ref[j, rows, :] = jnp.broadcast_to(m + jnp.log(den), (SPAN, HEAD_DIM))

        _for_residues(r, per, residue)

    in_specs = [_head_spec(r, nb, hp, kind, group, jj) for kind in kinds for jj in range(hp)]
    in_specs.append(pl.BlockSpec((hp, SPAN, 2 * SPAN), lambda j, n: (group * (HEADS_PER_GROUP // hp) + j, 0, 0)))
    out = pl.BlockSpec((hp, r * SPAN, HEAD_DIM), lambda j, n: (j, n, 0))
    return pl.pallas_call(
        body, name=f"attn_fwd_g{group}", grid=(HEADS_PER_GROUP // hp, nb),
        in_specs=in_specs, out_specs=[out] * 2,
        out_shape=[jax.ShapeDtypeStruct((HEADS_PER_GROUP, S, HEAD_DIM), F32)] * 2,
        compiler_params=_params(("parallel", "parallel"), VMEM_LIMIT),
    )(*([proj] * (len(in_specs) - 1)), bias)


_PROJ_PART = dict(q=0, qn=0, kp=1, kc=1, vp=2, vc=2)


def _head_spec(r, nb, hp, kind, group, jj):
    if kind in _PROJ_PART:
        base = (_PROJ_PART[kind] * N_GROUPS + group) * HEADS_PER_GROUP
    else:
        base = 0
    if kind.endswith("p"):
        row = lambda n: jnp.maximum(n - 1, 0)
    elif kind.endswith("n"):
        row = lambda n: jnp.minimum(n + 1, nb - 1)
    else:
        row = lambda n: n
    return pl.BlockSpec((r * SPAN, HEAD_DIM), lambda j, n: (row(n), base + j * hp + jj))


def _attn_merge_call(parts):
    S = parts[0].shape[1]

    def body(o1, s1, o2, s2, o3, s3, a_ref, ab_ref, lse_ref):
        for j in range(HEADS_PER_GROUP):
            sl = slice(j * HEAD_DIM, (j + 1) * HEAD_DIM)
            mx = jnp.maximum(jnp.maximum(s1[j], s2[j]), s3[j])
            w1 = jnp.exp(s1[j] - mx)
            w2 = jnp.exp(s2[j] - mx)
            w3 = jnp.exp(s3[j] - mx)
            den = w1 + w2 + w3
            a = (w1 * o1[j] + w2 * o2[j] + w3 * o3[j]) / den
            a_ref[:, sl] = a
            ab_ref[:, sl] = a.astype(BF16)
            lse_ref[:, sl] = mx + jnp.log(den)

    heads = pl.BlockSpec((HEADS_PER_GROUP, ROW_TILE, HEAD_DIM), lambda i: (0, i, 0))
    return pl.pallas_call(
        body, name="attn_merge", grid=(S // ROW_TILE,),
        in_specs=[heads] * 6, out_specs=[_row_spec(GROUP_WIDTH)] * 3,
        out_shape=[jax.ShapeDtypeStruct((S, GROUP_WIDTH), F32), jax.ShapeDtypeStruct((S, GROUP_WIDTH), BF16),
                   jax.ShapeDtypeStruct((S, GROUP_WIDTH), F32)],
        compiler_params=_params(("parallel",)),
    )(*parts)


def _attn_delta_call(a, da):
    S = a.shape[0]

    def body(a_ref, da_ref, d_ref):
        for j in range(HEADS_PER_GROUP):
            sl = slice(j * HEAD_DIM, (j + 1) * HEAD_DIM)
            d = jnp.sum(a_ref[:, sl] * da_ref[:, sl], axis=-1, keepdims=True)
            d_ref[:, sl] = jnp.broadcast_to(d, (ROW_TILE, HEAD_DIM))

    return pl.pallas_call(
        body, name="attn_delta", grid=(S // ROW_TILE,),
        in_specs=[_row_spec(GROUP_WIDTH)] * 2, out_specs=_row_spec(GROUP_WIDTH),
        out_shape=jax.ShapeDtypeStruct((S, GROUP_WIDTH), F32),
        compiler_params=_params(("parallel",)),
    )(a, da)


def _attn_bwd_call(proj, bias, da, lse, delta, group):
    S = proj.shape[0]
    r, nb, hp, per = _attn_plan(S, group)
    scale = HEAD_DIM ** -0.5
    kinds = ("q", "qn", "kp", "kc", "vp", "vc", "da", "dan", "lse", "lsen", "dl", "dln") if nb > 1 else ("q", "kc", "vc", "da", "lse", "dl")
    source = dict(da=da, dan=da, lse=lse, lsen=lse, dl=delta, dln=delta)

    def body(*refs):
        ins = {kind: refs[i * hp:(i + 1) * hp] for i, kind in enumerate(kinds)}
        b_ref, dq_ref, dk_ref, dv_ref, db_ref = refs[len(kinds) * hp:]
        n = pl.program_id(1)
        prev_ok, cur_ok, next_ok = _band_masks(n, nb)

        @pl.when(n == 0)
        def _():
            db_ref[...] = jnp.zeros_like(db_ref)

        band_ok = _wide_band_mask(n) if nb > 1 else cur_ok

        def residue(rho):
            rows = _residue_rows(rho, r)
            for j in range(hp):
                get = lambda kind: ins[kind][j][rows, :]
                q = get("q").astype(BF16)
                kc = get("kc").astype(BF16)
                vc = get("vc").astype(BF16)
                dav = get("da").astype(BF16)
                lse_q, dl_q = get("lse"), get("dl")
                if nb == 1:
                    pc = jnp.exp(jnp.where(cur_ok, _dot_nt(q, kc) * scale + b_ref[j, :, SPAN:], NEG_INF) - lse_q)
                    dsc = pc * (_dot_nt(dav, vc) - dl_q)
                    dsc_b = dsc.astype(BF16)
                    dq = _dot_nn(dsc_b, kc)
                    dk = _dot_tn(dsc_b, q)
                    dv = _dot_tn(pc.astype(BF16), dav)
                    db_ref[j, :, SPAN:] += dsc
                else:
                    qn = get("qn").astype(BF16)
                    dan = get("dan").astype(BF16)
                    keys = jnp.concatenate([get("kp").astype(BF16), kc], axis=0)
                    vals = jnp.concatenate([get("vp").astype(BF16), vc], axis=0)
                    wide = lambda t: jnp.concatenate([t, t], axis=1)
                    p = jnp.exp(jnp.where(band_ok, _dot_nt(q, keys) * scale + b_ref[j], NEG_INF) - wide(lse_q))
                    ds = p * (_dot_nt(dav, vals) - wide(dl_q))
                    dq = _dot_nn(ds.astype(BF16), keys)
                    db_ref[j] += ds
                    pn = jnp.exp(jnp.where(next_ok, _dot_nt(qn, kc) * scale + b_ref[j, :, :SPAN], NEG_INF) - get("lsen"))
                    dsn = pn * (_dot_nt(dan, vc) - get("dln"))
                    both = lambda cur_part, next_part: jnp.concatenate([cur_part.astype(BF16), next_part.astype(BF16)], axis=0)
                    dk = _dot_tn(both(ds[:, SPAN:], dsn), jnp.concatenate([q, qn], axis=0))
                    dv = _dot_tn(both(p[:, SPAN:], pn), jnp.concatenate([dav, dan], axis=0))
                dq_ref[j, rows, :] = dq * scale
                dk_ref[j, rows, :] = dk * scale
                dv_ref[j, rows, :] = dv

        _for_residues(r, per, residue)

    per_group = HEADS_PER_GROUP // hp
    band = (hp, SPAN, 2 * SPAN)
    in_specs = [_head_spec(r, nb, hp, kind, group, jj) for kind in kinds for jj in range(hp)]
    in_specs.append(pl.BlockSpec(band, lambda j, n: (group * per_group + j, 0, 0)))
    operands = [source.get(kind, proj) for kind in kinds for _ in range(hp)] + [bias]
    out = pl.BlockSpec((hp, r * SPAN, HEAD_DIM), lambda j, n: (j, n, 0))
    return pl.pallas_call(
        body, name=f"attn_bwd_g{group}", grid=(per_group, nb),
        in_specs=in_specs,
        out_specs=[out] * 3 + [pl.BlockSpec(band, lambda j, n: (j, 0, 0))],
        out_shape=[jax.ShapeDtypeStruct((HEADS_PER_GROUP, S, HEAD_DIM), F32)] * 3
        + [jax.ShapeDtypeStruct((HEADS_PER_GROUP, SPAN, 2 * SPAN), F32)],
        compiler_params=_params(("parallel", "arbitrary"), VMEM_LIMIT),
    )(*operands)


def _dproj_call(dqkv, tails):
    S = tails[0].shape[0]
    width = len(dqkv) * GROUP_WIDTH + sum(t.shape[1] for t in tails)

    def body(*refs):
        o_ref = refs[-1]
        col = 0
        for ref in refs[:len(dqkv)]:
            for j in range(HEADS_PER_GROUP):
                o_ref[:, col:col + HEAD_DIM] = ref[j].astype(BF16)
                col += HEAD_DIM
        for ref in refs[len(dqkv):-1]:
            o_ref[:, col:col + ref.shape[1]] = ref[...]
            col += ref.shape[1]

    heads = pl.BlockSpec((HEADS_PER_GROUP, ROW_TILE, HEAD_DIM), lambda i: (0, i, 0))
    return pl.pallas_call(
        body, name="dproj_assemble", grid=(S // ROW_TILE,),
        in_specs=[heads] * len(dqkv) + [_row_spec(t.shape[1]) for t in tails],
        out_specs=_row_spec(width), out_shape=jax.ShapeDtypeStruct((S, width), BF16),
        compiler_params=_params(("parallel",)),
    )(*dqkv, *tails)


def _tap_rows(xpad_ref, t0, k, width, pad):
    return xpad_ref[pl.ds(t0 + (pad - (width - 1 - k)), TIME_BLOCK), :]


def _conv_block(xpad_ref, t0, w_ref, width, pad):
    acc = None
    for k in range(width):
        term = w_ref[k:k + 1, :] * _tap_rows(xpad_ref, t0, k, width, pad)
        acc = term if acc is None else acc + term
    return acc


def _conv_transpose_block(dpad_ref, t0, w_ref, width):
    acc = None
    for k in range(width):
        term = w_ref[k:k + 1, :] * dpad_ref[pl.ds(t0 + (width - 1 - k), TIME_BLOCK), :]
        acc = term if acc is None else acc + term
    return acc


def _conv_weight_grad(xpad_ref, t0, dy, dw_ref, width, pad):
    for k in range(width):
        dw_ref[k:k + 1, :] += jnp.sum(dy * _tap_rows(xpad_ref, t0, k, width, pad), axis=0, keepdims=True)


def _time_loop(S, step):
    def it(tb, carry):
        step(pl.multiple_of(tb * TIME_BLOCK, TIME_BLOCK))
        return carry

    lax.fori_loop(0, S // TIME_BLOCK, it, 0)


def _conv_fwd_call(proj, col0, w, b):
    S = proj.shape[0]
    C = w.shape[1]
    nt = C // LANES
    v0, g0 = col0 // LANES, (col0 + C) // LANES

    def body(val_ref, gate_ref, w_ref, b_ref, o_ref, pad_ref):
        pad_ref[0:CONV_PAD, :] = jnp.zeros((CONV_PAD, LANES), F32)
        pad_ref[CONV_PAD:, :] = val_ref[...] * _sigmoid(gate_ref[...])

        def step(t0):
            o_ref[pl.ds(t0, TIME_BLOCK), :] = _conv_block(pad_ref, t0, w_ref, CONV_WIDTH, CONV_PAD) + b_ref[...]

        _time_loop(S, step)

    seq = lambda off: pl.BlockSpec((S, LANES), lambda i: (0, off + i))
    return pl.pallas_call(
        body, name="conv_module", grid=(nt,),
        in_specs=[seq(v0), seq(g0), pl.BlockSpec((CONV_WIDTH, LANES), lambda i: (0, i)), pl.BlockSpec((1, LANES), lambda i: (0, i))],
        out_specs=seq(0), out_shape=jax.ShapeDtypeStruct((S, C), F32),
        scratch_shapes=[pltpu.VMEM((S + CONV_PAD, LANES), F32)],
        compiler_params=_params(("parallel",)),
    )(proj, proj, w, b)


def _conv_bwd_call(proj, col0, w, dc1):
    S = proj.shape[0]
    C = w.shape[1]
    nt = C // LANES
    v0, g0 = col0 // LANES, (col0 + C) // LANES

    def body(val_ref, gate_ref, w_ref, dy_ref, dval_ref, dgate_ref, dw_ref, db_ref, xpad_ref, dpad_ref, dwacc_ref):
        xpad_ref[0:CONV_PAD, :] = jnp.zeros((CONV_PAD, LANES), F32)
        xpad_ref[CONV_PAD:, :] = val_ref[...] * _sigmoid(gate_ref[...])
        dpad_ref[0:S, :] = dy_ref[...]
        dpad_ref[S:, :] = jnp.zeros((CONV_PAD, LANES), F32)
        dwacc_ref[...] = jnp.zeros_like(dwacc_ref)

        def step(t0):
            rows = pl.ds(t0, TIME_BLOCK)
            _conv_weight_grad(xpad_ref, t0, dy_ref[rows, :], dwacc_ref, CONV_WIDTH, CONV_PAD)
            dc0 = _conv_transpose_block(dpad_ref, t0, w_ref, CONV_WIDTH)
            sg = _sigmoid(gate_ref[rows, :])
            dval_ref[rows, :] = (dc0 * sg).astype(BF16)
            dgate_ref[rows, :] = (dc0 * val_ref[rows, :] * sg * (1.0 - sg)).astype(BF16)

        _time_loop(S, step)
        dw_ref[...] = dwacc_ref[...]
        db_ref[...] = jnp.sum(dy_ref[...], axis=0, keepdims=True)

    seq = lambda off: pl.BlockSpec((S, LANES), lambda i: (0, off + i))
    return pl.pallas_call(
        body, name="conv_module_bwd", grid=(nt,),
        in_specs=[seq(v0), seq(g0), pl.BlockSpec((CONV_WIDTH, LANES), lambda i: (0, i)), seq(0)],
        out_specs=[seq(0), seq(0), pl.BlockSpec((CONV_PAD, LANES), lambda i: (0, i)), pl.BlockSpec((1, LANES), lambda i: (0, i))],
        out_shape=[jax.ShapeDtypeStruct((S, C), BF16), jax.ShapeDtypeStruct((S, C), BF16),
                   jax.ShapeDtypeStruct((CONV_PAD, C), F32), jax.ShapeDtypeStruct((1, C), F32)],
        scratch_shapes=[pltpu.VMEM((S + CONV_PAD, LANES), F32), pltpu.VMEM((S + CONV_PAD, LANES), F32),
                        pltpu.VMEM((CONV_PAD, LANES), F32)],
        compiler_params=_params(("parallel",)),
    )(proj, proj, w, dc1)


def _ffn_fwd_call(u, w, b):
    S, C2 = u.shape
    C = C2 // 2
    nt = C // LANES

    def body(ug_ref, uv_ref, wg_ref, wv_ref, bg_ref, bv_ref, f_ref, pg_ref, pv_ref):
        zeros = jnp.zeros((FFN_PAD, LANES), F32)
        pg_ref[0:FFN_PAD, :] = zeros
        pv_ref[0:FFN_PAD, :] = zeros
        pg_ref[FFN_PAD:, :] = ug_ref[...]
        pv_ref[FFN_PAD:, :] = uv_ref[...]

        def step(t0):
            cg = _conv_block(pg_ref, t0, wg_ref, FFN_CONV_WIDTH, FFN_PAD) + bg_ref[...]
            cv = _conv_block(pv_ref, t0, wv_ref, FFN_CONV_WIDTH, FFN_PAD) + bv_ref[...]
            f_ref[pl.ds(t0, TIME_BLOCK), :] = (_gelu(cg) * cv).astype(BF16)

        _time_loop(S, step)

    seq = lambda off: pl.BlockSpec((S, LANES), lambda i: (0, off + i))
    wsp = lambda off: pl.BlockSpec((FFN_CONV_WIDTH, LANES), lambda i: (0, off + i))
    bsp = lambda off: pl.BlockSpec((1, LANES), lambda i: (0, off + i))
    return pl.pallas_call(
        body, name="ffn_conv_geglu", grid=(nt,),
        in_specs=[seq(0), seq(nt), wsp(0), wsp(nt), bsp(0), bsp(nt)],
        out_specs=seq(0), out_shape=jax.ShapeDtypeStruct((S, C), BF16),
        scratch_shapes=[pltpu.VMEM((S + FFN_PAD, LANES), F32)] * 2,
        compiler_params=_params(("parallel",)),
    )(u, u, w, w, b, b)


def _ffn_bwd_call(u, w, b, df):
    S, C2 = u.shape
    C = C2 // 2
    nt = C // LANES

    def body(ug_ref, uv_ref, wg_ref, wv_ref, bg_ref, bv_ref, df_ref,
             du_ref, dwg_ref, dwv_ref, dbg_ref, dbv_ref,
             pg_ref, pv_ref, dg_ref, dv_ref, dwg_acc, dwv_acc, dbg_acc, dbv_acc):
        zeros = jnp.zeros((FFN_PAD, LANES), F32)
        pg_ref[0:FFN_PAD, :] = zeros
        pv_ref[0:FFN_PAD, :] = zeros
        pg_ref[FFN_PAD:, :] = ug_ref[...]
        pv_ref[FFN_PAD:, :] = uv_ref[...]
        dg_ref[S:, :] = zeros
        dv_ref[S:, :] = zeros
        dwg_acc[...] = jnp.zeros_like(dwg_acc)
        dwv_acc[...] = jnp.zeros_like(dwv_acc)
        dbg_acc[...] = jnp.zeros_like(dbg_acc)
        dbv_acc[...] = jnp.zeros_like(dbv_acc)

        def first(t0):
            rows = pl.ds(t0, TIME_BLOCK)
            cg = _conv_block(pg_ref, t0, wg_ref, FFN_CONV_WIDTH, FFN_PAD) + bg_ref[...]
            cv = _conv_block(pv_ref, t0, wv_ref, FFN_CONV_WIDTH, FFN_PAD) + bv_ref[...]
            dfb = df_ref[rows, :]
            gelu, gelu_grad = _gelu_and_grad(cg)
            dcg = dfb * cv * gelu_grad
            dcv = dfb * gelu
            dg_ref[rows, :] = dcg
            dv_ref[rows, :] = dcv
            _conv_weight_grad(pg_ref, t0, dcg, dwg_acc, FFN_CONV_WIDTH, FFN_PAD)
            _conv_weight_grad(pv_ref, t0, dcv, dwv_acc, FFN_CONV_WIDTH, FFN_PAD)
            dbg_acc[...] += jnp.sum(dcg, axis=0, keepdims=True)
            dbv_acc[...] += jnp.sum(dcv, axis=0, keepdims=True)

        def second(t0):
            rows = pl.ds(t0, TIME_BLOCK)
            du_ref[0, rows, :] = _conv_transpose_block(dg_ref, t0, wg_ref, FFN_CONV_WIDTH).astype(BF16)
            du_ref[1, rows, :] = _conv_transpose_block(dv_ref, t0, wv_ref, FFN_CONV_WIDTH).astype(BF16)

        _time_loop(S, first)
        _time_loop(S, second)
        dwg_ref[...] = dwg_acc[...]
        dwv_ref[...] = dwv_acc[...]
        dbg_ref[...] = dbg_acc[...]
        dbv_ref[...] = dbv_acc[...]

    seq = lambda off: pl.BlockSpec((S, LANES), lambda i: (0, off + i))
    wsp = lambda off: pl.BlockSpec((FFN_CONV_WIDTH, LANES), lambda i: (0, off + i))
    bsp = lambda off: pl.BlockSpec((1, LANES), lambda i: (0, off + i))
    return pl.pallas_call(
        body, name="ffn_conv_geglu_bwd", grid=(nt,),
        in_specs=[seq(0), seq(nt), wsp(0), wsp(nt), bsp(0), bsp(nt), seq(0)],
        out_specs=[pl.BlockSpec((2, S, LANES), lambda i: (0, 0, i)),
                   pl.BlockSpec((SUBLANES, LANES), lambda i: (0, i)), pl.BlockSpec((SUBLANES, LANES), lambda i: (0, i)),
                   bsp(0), bsp(0)],
        out_shape=[jax.ShapeDtypeStruct((2, S, C), BF16)] + [jax.ShapeDtypeStruct((SUBLANES, C), F32)] * 2
        + [jax.ShapeDtypeStruct((1, C), F32)] * 2,
        scratch_shapes=[pltpu.VMEM((S + FFN_PAD, LANES), F32)] * 4 + [pltpu.VMEM((SUBLANES, LANES), F32)] * 2
        + [pltpu.VMEM((1, LANES), F32)] * 2,
        compiler_params=_params(("parallel",)),
    )(u, u, w, w, b, b, df)


def _adamw(w_ref, g_ref, m_ref, v_ref, d_ref, mo_ref, vo_ref):
    gv = g_ref[...]
    mn = ADAM_B1 * m_ref[...] + (1.0 - ADAM_B1) * gv
    vn = ADAM_B2 * v_ref[...] + (1.0 - ADAM_B2) * (gv * gv)
    mo_ref[...] = mn
    vo_ref[...] = vn
    m_hat = mn * (1.0 / (1.0 - ADAM_B1 ** ADAM_STEP))
    v_hat = vn * (1.0 / (1.0 - ADAM_B2 ** ADAM_STEP))
    d_ref[...] = -ADAM_LR * (m_hat / (jnp.sqrt(v_hat) + ADAM_EPS) + ADAM_WD * w_ref[...])


def _adamw_call(w, g, m, v, name):
    R, C = w.shape
    tr = _row_tile(R, C)

    def body(w_ref, g_ref, m_ref, v_ref, go_ref, d_ref, mo_ref, vo_ref):
        go_ref[...] = g_ref[...]
        _adamw(w_ref, g_ref, m_ref, v_ref, d_ref, mo_ref, vo_ref)

    spec = pl.BlockSpec((tr, C), lambda i: (i, 0))
    return pl.pallas_call(
        body, name=name, grid=(R // tr,),
        in_specs=[spec] * 4, out_specs=[spec] * 4,
        out_shape=[jax.ShapeDtypeStruct((R, C), F32)] * 4,
        compiler_params=_params(("parallel",)),
    )(w, g, m, v)


def _adamw_small_call(ws, gs, ms, vs):
    n = len(ws)

    def body(*refs):
        w_refs, g_refs, m_refs, v_refs, d_refs, mo_refs, vo_refs = (refs[i * n:(i + 1) * n] for i in range(7))
        for i in range(n):
            _adamw(w_refs[i], g_refs[i], m_refs[i], v_refs[i], d_refs[i], mo_refs[i], vo_refs[i])

    whole = pl.BlockSpec(memory_space=pltpu.VMEM)
    outs = pl.pallas_call(
        body, name="adamw_small",
        in_specs=[whole] * (4 * n), out_specs=[whole] * (3 * n),
        out_shape=[jax.ShapeDtypeStruct(w.shape, F32) for w in ws] * 3,
    )(*ws, *gs, *ms, *vs)
    return outs[:n], outs[n:2 * n], outs[2 * n:]


def _position():
    return lax.axis_index("x"), lax.axis_index("y"), lax.axis_index("c")


def _chip_peers(x, y):
    return [(x, 1 - y), (1 - x, y), (1 - x, 1 - y)]


def _half_rows(ref, core, rows):
    h = rows // 2
    start = pl.multiple_of(core * h, 16)
    return ref.at[pl.ds(start, h), :] if len(ref.shape) == 2 else ref.at[:, pl.ds(start, h), :]


def _shard_half(ref, shard, core, rows):
    h = rows // 2
    return ref.at[shard, pl.ds(pl.multiple_of(core * h, 16), h), :]


ANY = pl.BlockSpec(memory_space=pl.ANY)


def _allgather_call(shards, whole):
    n, nw = len(shards), len(whole)
    outs_shape = [jax.ShapeDtypeStruct((N_CHIPS,) + s.shape, s.dtype) for s in shards + whole]

    def body(*refs):
        ins, outs = refs[:n + nw], refs[n + nw:2 * (n + nw)]
        send_sems, recv_sems, pass_send, pass_recv, own_send, own_recv = refs[2 * (n + nw):]
        x, y, c = _position()
        chip = 2 * x + y
        peers = _chip_peers(x, y)
        sent, local = [], []
        for i in range(n + nw):
            cp = pltpu.make_async_remote_copy(src_ref=ins[i], dst_ref=outs[i].at[chip], send_sem=own_send.at[i],
                                              recv_sem=own_recv.at[i], device_id=(x, y, 1 - c), device_id_type=MESH)
            cp.start()
            local.append(cp)
            rows = ins[i].shape[0]
            for k, (px, py) in enumerate(peers):
                if i < n:
                    src, dst = _half_rows(ins[i], c, rows), _shard_half(outs[i], chip, c, rows)
                else:
                    src, dst = ins[i], outs[i].at[chip]
                cp = pltpu.make_async_remote_copy(src_ref=src, dst_ref=dst, send_sem=send_sems.at[i, k],
                                                  recv_sem=recv_sems.at[i, k], device_id=(px, py, c), device_id_type=MESH)
                cp.start()
                sent.append(cp)
        passed = []
        for i in range(n + nw):
            rows = ins[i].shape[0]
            for k, (px, py) in enumerate(peers):
                landed = _shard_half(outs[i], 2 * px + py, c, rows) if i < n else outs[i].at[2 * px + py]
                pltpu.make_async_remote_copy(src_ref=landed, dst_ref=landed, send_sem=send_sems.at[i, k],
                                             recv_sem=recv_sems.at[i, k], device_id=(px, py, c), device_id_type=MESH).wait_recv()
                if i < n:
                    cp = pltpu.make_async_remote_copy(src_ref=landed, dst_ref=landed, send_sem=pass_send.at[i, k],
                                                      recv_sem=pass_recv.at[i, k], device_id=(x, y, 1 - c), device_id_type=MESH)
                    cp.start()
                    passed.append(cp)
        for cp in sent:
            cp.wait_send()
        for cp in passed:
            cp.wait()
        for cp in local:
            cp.wait()

    return pl.pallas_call(
        body, name="weight_allgather",
        in_specs=[ANY] * (n + nw), out_specs=[ANY] * (n + nw), out_shape=outs_shape,
        scratch_shapes=[pltpu.SemaphoreType.DMA((n + nw, 3)), pltpu.SemaphoreType.DMA((n + nw, 3)),
                        pltpu.SemaphoreType.DMA((n, 3)), pltpu.SemaphoreType.DMA((n, 3)),
                        pltpu.SemaphoreType.DMA((n + nw,)), pltpu.SemaphoreType.DMA((n + nw,))],
    )(*shards, *whole)


HBM_SPEC = pl.BlockSpec(memory_space=pltpu.HBM)
SEM_SPEC = pl.BlockSpec(memory_space=pltpu.SEMAPHORE)
DATAFLOW = pltpu.SideEffectType.DATAFLOW_SIDE_EFFECTING


def _in_hbm(a):
    return pltpu.with_memory_space_constraint(a, pltpu.HBM)


def _split_start(name, groups, after):
    spans, arrays = [], []
    for srcs, lands, _, _ in groups:
        spans.append((len(arrays), len(srcs), len(lands)))
        arrays += list(srcs) + list(lands)
    na, ng = len(arrays), len(groups)

    def body(*refs):
        sems, token = refs[na + 1:na + 1 + 2 * ng], refs[-1]
        for g, (_, _, _, copies) in enumerate(groups):
            off, ns, nl = spans[g]
            for src, dst, dev, idx in copies(refs[off:off + ns], refs[off + ns:off + ns + nl]):
                pltpu.make_async_remote_copy(src_ref=src, dst_ref=dst, send_sem=sems[2 * g].at[idx], recv_sem=sems[2 * g + 1].at[idx],
                                             device_id=dev, device_id_type=MESH).start()
        token[...] = jnp.zeros_like(token)

    outs = pl.pallas_call(
        body, name=name,
        in_specs=[HBM_SPEC] * na + [ANY],
        out_specs=[SEM_SPEC] * (2 * ng) + [HBM_SPEC] * na + [pl.BlockSpec(memory_space=pltpu.VMEM)],
        out_shape=[pltpu.SemaphoreType.DMA((n_sems,)) for _, _, n_sems, _ in groups for _ in range(2)]
        + [pltpu.HBM(a.shape, a.dtype) for a in arrays] + [jax.ShapeDtypeStruct((SUBLANES, LANES), F32)],
        input_output_aliases={i: 2 * ng + i for i in range(na)},
        compiler_params=pltpu.CompilerParams(has_side_effects=DATAFLOW),
    )(*[_in_hbm(a) for a in arrays], after)
    started = []
    for g, (off, ns, nl) in enumerate(spans):
        thru = outs[2 * ng + off:2 * ng + off + ns + nl]
        started.append(dict(send=outs[2 * g], recv=outs[2 * g + 1], srcs=list(thru[:ns]), lands=list(thru[ns:]),
                            tile=outs[-1], token=outs[-1][0, 0]))
    return started


def _split_wait(name, started, copies, after):
    n, m = len(started["srcs"]), len(started["lands"])

    def body(*refs):
        src_refs, land_refs = refs[:n], refs[n:n + m]
        send_sem, recv_sem = refs[n + m], refs[n + m + 1]
        for src, dst, dev, idx in copies(src_refs, land_refs):
            cp = pltpu.make_async_remote_copy(src_ref=src, dst_ref=dst, send_sem=send_sem.at[idx], recv_sem=recv_sem.at[idx],
                                              device_id=dev, device_id_type=MESH)
            cp.wait_send()
            cp.wait_recv()

    arrays = started["srcs"] + started["lands"]
    outs = pl.pallas_call(
        body, name=name,
        in_specs=[HBM_SPEC] * (n + m) + [SEM_SPEC, SEM_SPEC, ANY],
        out_specs=[HBM_SPEC] * (n + m),
        out_shape=[pltpu.HBM(a.shape, a.dtype) for a in arrays],
        input_output_aliases={i: i for i in range(n + m)},
        compiler_params=pltpu.CompilerParams(has_side_effects=DATAFLOW),
    )(*arrays, started["send"], started["recv"], after)
    return list(outs)


def _gather_copies(srcs, lands):
    x, y, c = _position()
    chip = 2 * x + y
    targets = [(px, py, c) for px, py in _chip_peers(x, y)] + [(x, y, 1 - c)]
    return [(s, l.at[chip], dev, len(targets) * i + k) for i, (s, l) in enumerate(zip(srcs, lands)) for k, dev in enumerate(targets)]


def _sibling_copies(srcs, lands):
    x, y, c = _position()
    return [(_half_rows(srcs[0], 1 - c, srcs[0].shape[1]), lands[0], (x, y, 1 - c), 0)]


def _exchange_copies(srcs, lands):
    x, y, c = _position()
    return [(srcs[0].at[2 * px + py], lands[0].at[k], (px, py, c), k) for k, (px, py) in enumerate(_chip_peers(x, y))]


def _pair_sum_call(grad, recv, core, name):
    _, h, B = recv.shape
    tr = _row_tile(h, B)

    def body(core_ref, g_ref, r_ref, o_ref, ob_ref):
        s = g_ref[...] + r_ref[...]
        o_ref[...] = s
        ob_ref[...] = s.astype(BF16)

    g_spec = pl.BlockSpec((None, tr, B), lambda q, i, core_ref: (q, core_ref[0] * (h // tr) + i, 0))
    spec = pl.BlockSpec((None, tr, B), lambda q, i, core_ref: (q, i, 0))
    return pl.pallas_call(
        body, name=name,
        grid_spec=pltpu.PrefetchScalarGridSpec(num_scalar_prefetch=1, grid=(N_CHIPS, h // tr), in_specs=[g_spec, spec],
                                               out_specs=[spec, spec]),
        out_shape=[jax.ShapeDtypeStruct(recv.shape, F32), jax.ShapeDtypeStruct(recv.shape, BF16)],
        compiler_params=_params(("parallel", "parallel")),
    )(core, grad, recv)


def _chip_sum_call(partial, recv, chip_core, name):
    _, h, B = recv.shape
    tr = _row_tile(h, B)

    def body(cc_ref, p_ref, r_ref, o_ref):
        o_ref[...] = ((p_ref[...] + r_ref[0].astype(F32)) + r_ref[1].astype(F32)) + r_ref[2].astype(F32)

    return pl.pallas_call(
        body, name=name,
        grid_spec=pltpu.PrefetchScalarGridSpec(
            num_scalar_prefetch=1, grid=(h // tr,),
            in_specs=[pl.BlockSpec((None, tr, B), lambda i, cc_ref: (cc_ref[0], i, 0)),
                      pl.BlockSpec((3, tr, B), lambda i, cc_ref: (0, i, 0))],
            out_specs=pl.BlockSpec((tr, B), lambda i, cc_ref: (cc_ref[1] * (h // tr) + i, 0))),
        out_shape=jax.ShapeDtypeStruct((2 * h, B), F32),
        compiler_params=_params(("parallel",)),
    )(chip_core, partial, recv)


def _sibling_assemble_call(shards, name="grad_sibling_assemble"):
    n = len(shards)

    def body(*refs):
        ins, outs = refs[:n], refs[n:2 * n]
        send_sems, recv_sems = refs[2 * n:]
        x, y, c = _position()
        copies = []
        for i in range(n):
            rows = shards[i].shape[0]
            cp = pltpu.make_async_remote_copy(src_ref=_half_rows(ins[i], c, rows), dst_ref=_half_rows(outs[i], c, rows),
                                              send_sem=send_sems.at[i], recv_sem=recv_sems.at[i],
                                              device_id=(x, y, 1 - c), device_id_type=MESH)
            cp.start()
            copies.append(cp)
        for cp in copies:
            cp.wait()

    return pl.pallas_call(
        body, name=name,
        in_specs=[ANY] * n, out_specs=[ANY] * n,
        out_shape=[jax.ShapeDtypeStruct(s.shape, F32) for s in shards],
        input_output_aliases={i: i for i in range(n)},
        scratch_shapes=[pltpu.SemaphoreType.DMA((n,)), pltpu.SemaphoreType.DMA((n,))],
    )(*shards)


N_DEVICES = 8


def _allsum_copies(srcs, lands):
    x, y, c = _position()
    me = 4 * x + 2 * y + c
    out = []
    for k in range(1, N_DEVICES):
        peer = (1 - x if k & 4 else x, 1 - y if k & 2 else y, 1 - c if k & 1 else c)
        out.append((srcs[0], lands[0].at[me], peer, k - 1))
    return out


def _ordered_sum_call(mine, landed, me):
    rows = mine.shape[0]

    def body(me_ref, x_ref, l_ref, o_ref):
        acc = jnp.where(me_ref[0] == 0, x_ref[...], l_ref[0])
        for d in range(1, N_DEVICES):
            acc = acc + jnp.where(me_ref[0] == d, x_ref[...], l_ref[d])
        o_ref[...] = acc

    return pl.pallas_call(
        body, name="small_grad_sum",
        in_specs=[pl.BlockSpec(memory_space=pltpu.SMEM), pl.BlockSpec(memory_space=pltpu.VMEM), pl.BlockSpec(memory_space=pltpu.VMEM)],
        out_specs=pl.BlockSpec(memory_space=pltpu.VMEM),
        out_shape=jax.ShapeDtypeStruct((rows, LANES), F32),
    )(me, mine, landed)


def _pack(arrays):
    flat = jnp.concatenate([a.reshape(-1).astype(F32) for a in arrays])
    rows = -(-flat.shape[0] // LANES)
    rows = -(-rows // SUBLANES) * SUBLANES
    flat = jnp.pad(flat, (0, rows * LANES - flat.shape[0]))
    return flat.reshape(rows, LANES)


def _unpack(packed, shapes):
    flat = packed.reshape(-1)
    out, off = [], 0
    for shp in shapes:
        size = int(np.prod(shp))
        out.append(flat[off:off + size].reshape(shp))
        off += size
    return out


def _local_step(xs, target, P, late_weights, on_grad):
    S, D = xs.shape
    qkv_width = 3 * N_HEADS * HEAD_DIM
    glu_col0, gate_col0 = qkv_width, qkv_width + 2 * D
    shard_major = lambda g: g.reshape(N_CHIPS, g.shape[0] // N_CHIPS, g.shape[1])

    h1 = _rms_fwd_call(xs, P["norm_mix_pre"])
    proj = _matmul(h1, P["w_in"], "nn", "proj_in")
    buckets = _bucket_tables()
    bias = _bias_table_call(P["rel_bias"], buckets)
    parts = []
    for g in range(N_GROUPS):
        parts += _attn_fwd_call(proj, bias, g)
    a, a_bf, lse = _attn_merge_call(parts)
    P = dict(P, **late_weights("mix", a_bf))
    y_a = _matmul(a_bf, P["w_attn_out"], "nn", "attn_out")
    c1 = _conv_fwd_call(proj, glu_col0, P["conv_dw_w"], P["conv_dw_b"])
    cact = _ln_silu_call(c1, P["conv_ln_g"], P["conv_ln_b"])
    y_c = _matmul(cact, P["conv_pw_w"], "nn", "conv_pw")
    mixed = _mix_call(proj, gate_col0, P["b_gate"], y_a, y_c)
    out = _matmul(mixed, P["w_out"], "nn", "mix_out")
    x1, h2 = _res1_call(xs, out, P["norm_mix_post"], P["norm_ffn_pre"])
    P = dict(P, **late_weights("ffn", h2))
    u = _matmul(h2, P["w_up"], "nn", "ffn_up")
    f = _ffn_fwd_call(u, P["ffn_conv_w"], P["ffn_conv_b"])
    yff = _matmul(f, P["w_down"], "nn", "ffn_down")
    loss_tile, dx2, dyff, dg_ffn_post = _loss_call(yff, x1, P["norm_ffn_post"], target)

    G = {}
    G["norm_ffn_post"] = dg_ffn_post
    zero = on_grad("w_down", shard_major(_matmul(f, dyff, "tn", "ffn_down_dw")))
    df = _matmul(dyff, P["w_down"], "nt", "ffn_down_dx")
    du, dwg, dwv, dbg, dbv = _ffn_bwd_call(u, P["ffn_conv_w"], P["ffn_conv_b"] + zero, df)
    G["ffn_conv_w"] = jnp.concatenate([dwg[:FFN_CONV_WIDTH], dwv[:FFN_CONV_WIDTH]], axis=1)
    G["ffn_conv_b"] = jnp.concatenate([dbg, dbv], axis=1)
    zero = on_grad("w_up", _matmul(h2, du, "tn", "ffn_up_dw", out_shards=True))
    dh2 = _matmul(du, P["w_up"], "nt", "ffn_up_dx")
    dx1, dout, G["norm_ffn_pre"], G["norm_mix_post"] = _mid_bwd_call(x1, P["norm_ffn_pre"] + zero, dh2, dx2, out, P["norm_mix_post"])
    zero = on_grad("w_out", shard_major(_matmul(mixed, dout, "tn", "mix_out_dw")))
    dmixed = _matmul(dout, P["w_out"], "nt", "mix_out_dx")
    dya, dyc, dga, dgc, dba, dbc = _mix_bwd_call(dmixed, proj, gate_col0, P["b_gate"] + zero, y_a, y_c)
    G["b_gate"] = jnp.concatenate([dba, dbc], axis=1)
    zero = on_grad("w_attn_out", _matmul(a_bf, dya, "tn", "attn_out_dw", out_shards=True))
    zero = zero + on_grad("conv_pw_w", shard_major(_matmul(cact, dyc, "tn", "conv_pw_dw")))
    da = _matmul(dya, P["w_attn_out"], "nt", "attn_out_dx")
    dcact = _matmul(dyc, P["conv_pw_w"], "nt", "conv_pw_dx")
    dc1, G["conv_ln_g"], G["conv_ln_b"] = _ln_silu_bwd_call(c1, P["conv_ln_g"] + zero, P["conv_ln_b"], dcact)
    dval, dgate, dw_dw, G["conv_dw_b"] = _conv_bwd_call(proj, glu_col0, P["conv_dw_w"], dc1)
    G["conv_dw_w"] = dw_dw[:CONV_WIDTH]
    delta = _attn_delta_call(a, da)
    dqs, dks, dvs, dbs = [], [], [], []
    for g in range(N_GROUPS):
        dq, dk, dv, db = _attn_bwd_call(proj, bias, da, lse, delta, g)
        dqs.append(dq)
        dks.append(dk)
        dvs.append(dv)
        dbs.append(db)
    G["rel_bias"] = _bias_grad_call(jnp.concatenate(dbs, axis=0), buckets)
    dproj = _dproj_call(dqs + dks + dvs, [dval, dgate, dga, dgc])
    zero = on_grad("w_in", _matmul(h1, dproj, "tn", "proj_in_dw", out_shards=True))
    dh1 = _matmul(dproj, P["w_in"], "nt", "proj_in_dx")
    zero = zero + on_grad(None, dh1)
    grad_x, G["norm_mix_pre"] = _in_bwd_call(xs, P["norm_mix_pre"] + zero, dh1, dx1)
    return loss_tile, grad_x, G


def kernel(x, w_in, b_gate, rel_bias, w_attn_out, conv_dw_w, conv_dw_b, conv_ln_g, conv_ln_b, conv_pw_w, w_out, norm_mix_pre, norm_mix_post, norm_ffn_pre, norm_ffn_post, w_up, ffn_conv_w, ffn_conv_b, w_down, loss_target, m_w_in, m_b_gate, m_rel_bias, m_w_attn_out, m_conv_dw_w, m_conv_dw_b, m_conv_ln_g, m_conv_ln_b, m_conv_pw_w, m_w_out, m_norm_mix_pre, m_norm_mix_post, m_norm_ffn_pre, m_norm_ffn_post, m_w_up, m_ffn_conv_w, m_ffn_conv_b, m_w_down, v_w_in, v_b_gate, v_rel_bias, v_w_attn_out, v_conv_dw_w, v_conv_dw_b, v_conv_ln_g, v_conv_ln_b, v_conv_pw_w, v_w_out, v_norm_mix_pre, v_norm_mix_post, v_norm_ffn_pre, v_norm_ffn_post, v_w_up, v_ffn_conv_w, v_ffn_conv_b, v_w_down):
    weights = dict(w_in=w_in, b_gate=b_gate, rel_bias=rel_bias, w_attn_out=w_attn_out, conv_dw_w=conv_dw_w, conv_dw_b=conv_dw_b,
                   conv_ln_g=conv_ln_g, conv_ln_b=conv_ln_b, conv_pw_w=conv_pw_w, w_out=w_out, norm_mix_pre=norm_mix_pre,
                   norm_mix_post=norm_mix_post, norm_ffn_pre=norm_ffn_pre, norm_ffn_post=norm_ffn_post, w_up=w_up,
                   ffn_conv_w=ffn_conv_w, ffn_conv_b=ffn_conv_b, w_down=w_down)
    m_in = dict(w_in=m_w_in, b_gate=m_b_gate, rel_bias=m_rel_bias, w_attn_out=m_w_attn_out, conv_dw_w=m_conv_dw_w,
                conv_dw_b=m_conv_dw_b, conv_ln_g=m_conv_ln_g, conv_ln_b=m_conv_ln_b, conv_pw_w=m_conv_pw_w, w_out=m_w_out,
                norm_mix_pre=m_norm_mix_pre, norm_mix_post=m_norm_mix_post, norm_ffn_pre=m_norm_ffn_pre,
                norm_ffn_post=m_norm_ffn_post, w_up=m_w_up, ffn_conv_w=m_ffn_conv_w, ffn_conv_b=m_ffn_conv_b, w_down=m_w_down)
    v_in = dict(w_in=v_w_in, b_gate=v_b_gate, rel_bias=v_rel_bias, w_attn_out=v_w_attn_out, conv_dw_w=v_conv_dw_w,
                conv_dw_b=v_conv_dw_b, conv_ln_g=v_conv_ln_g, conv_ln_b=v_conv_ln_b, conv_pw_w=v_conv_pw_w, w_out=v_w_out,
                norm_mix_pre=v_norm_mix_pre, norm_mix_post=v_norm_mix_post, norm_ffn_pre=v_norm_ffn_pre,
                norm_ffn_post=v_norm_ffn_post, w_up=v_w_up, ffn_conv_w=v_ffn_conv_w, ffn_conv_b=v_ffn_conv_b, w_down=v_w_down)
    names = list(weights)
    xi, yi, ci = _position()
    chip = 2 * xi + yi
    core_arr = jnp.reshape(ci, (1,)).astype(jnp.int32)

    xs = x[0]
    target = loss_target[0]
    S, D = xs.shape

    big = ["w_in", "w_attn_out", "conv_pw_w", "w_out", "w_up", "w_down"]
    row_sharded = ("conv_pw_w", "w_out", "w_down")
    bf16_shard = {k: weights[k][0].astype(BF16) for k in big}
    natural = lambda k, g: g.reshape(-1, g.shape[2]) if k in row_sharded else g
    w_in_full, dw4, fc4 = _allgather_call([bf16_shard["w_in"]], [conv_dw_w[0], ffn_conv_w[0]])
    late_sets = dict(mix=["w_attn_out", "conv_pw_w", "w_out"], ffn=["w_up", "w_down"])
    late_groups = []
    for keys in late_sets.values():
        srcs = [bf16_shard[k] for k in keys]
        late_groups.append((srcs, [lax.empty((N_CHIPS,) + s.shape, BF16) for s in srcs], 4 * len(keys), _gather_copies))
    started = dict(zip(late_sets, _split_start("gather_late_start", late_groups, w_in_full)))
    launched = started["mix"]["token"]

    def late_weights(tag, after):
        landed = _split_wait(f"gather_{tag}_wait", started[tag], _gather_copies, after)[len(late_sets[tag]):]
        return {k: natural(k, g) for k, g in zip(late_sets[tag], landed)}

    chip_core = jnp.stack([chip, ci]).astype(jnp.int32)
    exchanging, pending = {}, {}

    def launch(tag, g3, after):
        keys, groups, partial = [], [], {}
        for k in list(exchanging):
            gk, r1 = _split_wait(f"sibling_exchange_wait_{k}", exchanging.pop(k), _sibling_copies, after)
            partial[k], s16 = _pair_sum_call(gk, r1, core_arr, f"pair_sum_{k}")
            keys.append(k)
            groups.append(([s16], [lax.empty((3,) + s16.shape[1:], BF16)], 3, _exchange_copies))
        if g3 is not None:
            groups.append(([g3], [lax.empty((N_CHIPS, g3.shape[1] // 2, g3.shape[2]), F32)], 1, _sibling_copies))
        begun = _split_start(f"grad_exchange_start_{tag}", groups, core_arr)
        for k, st in zip(keys, begun):
            pending[k] = (partial[k], st)
        if g3 is not None:
            exchanging[tag] = begun[-1]
        return begun[0]["token"]

    def on_grad(k, g3):
        if k is None:
            return launch("last", None, g3[:SUBLANES, :LANES])
        return launch(k, g3, g3[0, :SUBLANES, :LANES])

    def finish(keys, after, tag):
        halves = []
        for k in keys:
            s32, st = pending[k]
            recv2 = _split_wait(f"chip_exchange_wait_{k}", st, _exchange_copies, after)[1]
            halves.append(_chip_sum_call(s32, recv2, chip_core, f"chip_sum_{k}"))
        return dict(zip(keys, _sibling_assemble_call(halves, f"grad_sibling_assemble_{tag}")))

    P = dict(w_in=w_in_full, conv_dw_w=jnp.concatenate(list(dw4), axis=1), ffn_conv_w=jnp.concatenate(list(fc4), axis=1),
             b_gate=b_gate, rel_bias=rel_bias, conv_dw_b=conv_dw_b, conv_ln_g=conv_ln_g, conv_ln_b=conv_ln_b,
             norm_mix_pre=norm_mix_pre + launched, norm_mix_post=norm_mix_post, norm_ffn_pre=norm_ffn_pre,
             norm_ffn_post=norm_ffn_post, ffn_conv_b=ffn_conv_b)
    loss_tile, grad_x, G = _local_step(xs, target, P, late_weights, on_grad)

    small = [k for k in names if k not in big]
    packed = _pack([loss_tile[:1]] + [G[k] for k in small])
    (allsum,) = _split_start("small_grad_allsum_start",
                             [([packed], [jnp.zeros((N_DEVICES,) + packed.shape, F32)], N_DEVICES - 1, _allsum_copies)], core_arr)

    reduced, grads, deltas, new_m, new_v = {}, {}, {}, {}, {}

    def update(keys):
        for k in keys:
            gk, d, mn, vn = _adamw_call(weights[k][0], reduced[k], m_in[k][0], v_in[k][0], f"adamw_{k}")
            grads[k], deltas[k], new_m[k], new_v[k] = gk[None], d[None], mn[None], vn[None]

    others = [k for k in big if k != "w_in"]
    reduced.update(finish(others, allsum["tile"], "others"))
    update(others)
    reduced.update(finish(["w_in"], deltas["w_up"], "w_in"))
    update(["w_in"])

    me = jnp.reshape(4 * xi + 2 * yi + ci, (1,)).astype(jnp.int32)
    mine, landed = _split_wait("small_grad_allsum_wait", allsum, _allsum_copies, deltas["w_in"])
    summed_block = _ordered_sum_call(mine, landed, me)
    loss_row, *summed = _unpack(summed_block, [(1, LANES)] + [G[k].shape for k in small])
    loss = loss_row[0, 0]
    for k, gsum in zip(small, summed):
        if k in ("conv_dw_w", "ffn_conv_w"):
            cols = weights[k].shape[2]
            reduced[k] = lax.dynamic_slice_in_dim(gsum, chip * cols, cols, axis=1)
        else:
            reduced[k] = gsum
    for k in small:
        grads[k] = reduced[k].reshape(weights[k].shape)
    ds, mns, vns = _adamw_small_call([weights[k] for k in small], [grads[k] for k in small],
                                     [m_in[k] for k in small], [v_in[k] for k in small])
    deltas.update(zip(small, ds))
    new_m.update(zip(small, mns))
    new_v.update(zip(small, vns))

    return (loss, grad_x[None], *[grads[k] for k in names], *[deltas[k] for k in names],
            *[new_m[k] for k in names], *[new_v[k] for k in names])
```

```python
import functools
import math

import jax
import jax.numpy as jnp
import numpy as np
from jax import lax
from jax.experimental import pallas as pl
from jax.experimental.pallas import tpu as pltpu

F32 = jnp.float32
BF16 = jnp.bfloat16
MESH = pl.DeviceIdType.MESH

HEAD_DIM = 128
HEADS_PER_GROUP = 4
DILATED_PATTERNS = ((128, 1), (512, 4), (2048, 16))
N_GROUPS = 3
N_HEADS = N_GROUPS * HEADS_PER_GROUP
SPAN = 128
GROUP_WIDTH = HEADS_PER_GROUP * HEAD_DIM
CONV_WIDTH = 31
FFN_CONV_WIDTH = 3
N_BUCKETS = 32
MAX_DISTANCE = 2048
RMS_EPS = 1e-6
LN_EPS = 1e-5
NEG_INF = -1e30
ADAM_LR = 0.001
ADAM_B1 = 0.9
ADAM_B2 = 0.999
ADAM_EPS = 1e-08
ADAM_WD = 0.01
ADAM_STEP = 10

LANES = 128
SUBLANES = 8
ROW_TILE = 512
GATE_ROWS, GATE_COLS = 512, 512
TIME_BLOCK = 128
CONV_PAD = 32
FFN_PAD = 8
VMEM_LIMIT = 56 << 20


def _params(sem=None, vmem=None):
    kw = {}
    if sem is not None:
        kw["dimension_semantics"] = sem
    if vmem is not None:
        kw["vmem_limit_bytes"] = vmem
    return pltpu.CompilerParams(**kw)


def _pick(n, cands):
    for c in cands:
        if n % c == 0:
            return c
    return n


ELEMENTWISE_TILE_BYTES = 3 << 19


def _row_tile(rows, cols):
    for align in (16, SUBLANES):
        fits = [t for t in range(align, rows + 1, align) if rows % t == 0 and t * cols * 4 <= ELEMENTWISE_TILE_BYTES]
        if fits:
            return max(fits)
    return SUBLANES


N_CHIPS = 4
M_TILES = (1024, 1408, 512, 256, 128)
N_TILES = (1024, 512, 1408, 256, 128)
K_TILES = (2176, 2048, 1408, 1024, 512, 256, 128)


def _matmul(a, b, mode, name, out_shards=False, tm=None):
    assert a.dtype == BF16 and b.dtype == BF16, (name, a.dtype, b.dtype)
    b3 = b.ndim == 3
    tn = tk = None
    halves = None
    if mode == "nn":
        M, K = a.shape
        N = b.shape[-1] * (N_CHIPS if b3 else 1)
        tn = b.shape[-1] if b3 else None
    elif mode == "nt":
        if a.ndim == 3:
            halves = a.shape[2]
        M, K = a.shape[-2], a.shape[-1] * (a.shape[0] if a.ndim == 3 else 1)
        N = b.shape[-2]
        tk = b.shape[-1] if b3 else None
    else:
        if b3:
            halves = b.shape[2]
        K, M = a.shape
        N = b.shape[-1] * (b.shape[0] if b3 else 1)
        tn = N // N_CHIPS if out_shards else None
    tm = tm or _pick(M, M_TILES)
    tn = tn or _pick(N, N_TILES)
    tk = tk or _pick(K, K_TILES)
    nk = K // tk
    dn = {"nn": (((1,), (0,)), ((), ())), "nt": (((1,), (1,)), ((), ())), "tn": (((0,), (0,)), ((), ()))}[mode]

    def body(a_ref, b_ref, o_ref):
        if nk == 1:
            o_ref[...] = lax.dot_general(a_ref[...], b_ref[...], dn, preferred_element_type=F32)
        else:
            @pl.when(pl.program_id(2) == 0)
            def _():
                o_ref[...] = jnp.zeros_like(o_ref)

            o_ref[...] += lax.dot_general(a_ref[...], b_ref[...], dn, preferred_element_type=F32)

    if mode == "tn":
        a_spec = pl.BlockSpec((tk, tm), lambda i, j, k: (k, i))
    elif halves:
        per = halves // tk
        a_spec = pl.BlockSpec((None, tm, tk), lambda i, j, k: (k // per, i, k % per))
    else:
        a_spec = pl.BlockSpec((tm, tk), lambda i, j, k: (i, k))
    if mode == "nn":
        b_spec = pl.BlockSpec((None, tk, tn), lambda i, j, k: (j, k, 0)) if b3 else pl.BlockSpec((tk, tn), lambda i, j, k: (k, j))
    elif mode == "nt":
        b_spec = pl.BlockSpec((None, tn, tk), lambda i, j, k: (k, j, 0)) if b3 else pl.BlockSpec((tn, tk), lambda i, j, k: (j, k))
    elif halves:
        per = halves // tn
        b_spec = pl.BlockSpec((None, tk, tn), lambda i, j, k: (j // per, k, j % per))
    else:
        b_spec = pl.BlockSpec((tk, tn), lambda i, j, k: (k, j))
    if out_shards:
        out_spec = pl.BlockSpec((None, tm, tn), lambda i, j, k: (j, i, 0))
        out_shape = jax.ShapeDtypeStruct((N_CHIPS, M, tn), F32)
    else:
        out_spec = pl.BlockSpec((tm, tn), lambda i, j, k: (i, j))
        out_shape = jax.ShapeDtypeStruct((M, N), F32)
    return pl.pallas_call(
        body, name=name, grid=(M // tm, N // tn, nk),
        in_specs=[a_spec, b_spec], out_specs=out_spec, out_shape=out_shape,
        compiler_params=_params(("parallel", "parallel", "arbitrary"), VMEM_LIMIT),
    )(a, b)


def _rms(x, g):
    r = lax.rsqrt(jnp.mean(x * x, axis=-1, keepdims=True) + RMS_EPS)
    return x * r * g


def _rms_bwd(x, g, dy):
    r = lax.rsqrt(jnp.mean(x * x, axis=-1, keepdims=True) + RMS_EPS)
    n = x * r
    dn = dy * g
    dx = r * (dn - n * jnp.mean(dn * n, axis=-1, keepdims=True))
    return dx, jnp.sum(dy * n, axis=0, keepdims=True)


def _sigmoid(x):
    return 1.0 / (1.0 + jnp.exp(-x))


_GELU_C = math.sqrt(2.0 / math.pi)


def _gelu(x):
    return 0.5 * x * (1.0 + jnp.tanh(_GELU_C * (x + 0.044715 * x * x * x)))


def _gelu_and_grad(x):
    x2 = x * x
    t = jnp.tanh(_GELU_C * x * (1.0 + 0.044715 * x2))
    half = 0.5 * (1.0 + t)
    return x * half, half + (0.5 * _GELU_C) * x * (1.0 - t * t) * (1.0 + (3.0 * 0.044715) * x2)


def _row_spec(width, col_block=0):
    return pl.BlockSpec((ROW_TILE, width), lambda i: (i, col_block))


def _vec_spec(width, col_block=0):
    return pl.BlockSpec((1, width), lambda i: (0, col_block))


def _accumulate(ref, part):
    @pl.when(pl.program_id(0) == 0)
    def _():
        ref[...] = part

    @pl.when(pl.program_id(0) > 0)
    def _():
        ref[...] += part


def _rms_fwd_call(x, g):
    S, D = x.shape

    def body(x_ref, g_ref, h_ref):
        h_ref[...] = _rms(x_ref[...], g_ref[...]).astype(BF16)

    return pl.pallas_call(
        body, name="rms_mix_pre", grid=(S // ROW_TILE,),
        in_specs=[_row_spec(D), _vec_spec(D)], out_specs=_row_spec(D),
        out_shape=jax.ShapeDtypeStruct((S, D), BF16),
        compiler_params=_params(("parallel",)),
    )(x, g)


def _ln_silu_call(c1, g, b):
    S, C = c1.shape

    def body(c_ref, g_ref, b_ref, o_ref):
        xv = c_ref[...]
        mu = jnp.mean(xv, axis=-1, keepdims=True)
        xc = xv - mu
        var = jnp.mean(xc * xc, axis=-1, keepdims=True)
        z = xc * lax.rsqrt(var + LN_EPS) * g_ref[...] + b_ref[...]
        o_ref[...] = (z * _sigmoid(z)).astype(BF16)

    return pl.pallas_call(
        body, name="conv_ln_silu", grid=(S // ROW_TILE,),
        in_specs=[_row_spec(C), _vec_spec(C), _vec_spec(C)], out_specs=_row_spec(C),
        out_shape=jax.ShapeDtypeStruct((S, C), BF16),
        compiler_params=_params(("parallel",)),
    )(c1, g, b)


def _ln_silu_bwd_call(c1, g, b, dc):
    S, C = c1.shape

    def body(c_ref, g_ref, b_ref, dc_ref, dx_ref, dg_ref, db_ref):
        xv = c_ref[...]
        mu = jnp.mean(xv, axis=-1, keepdims=True)
        xc = xv - mu
        rs = lax.rsqrt(jnp.mean(xc * xc, axis=-1, keepdims=True) + LN_EPS)
        xh = xc * rs
        z = xh * g_ref[...] + b_ref[...]
        sg = _sigmoid(z)
        dz = dc_ref[...] * (sg * (1.0 + z * (1.0 - sg)))
        dxh = dz * g_ref[...]
        dx_ref[...] = rs * (dxh - jnp.mean(dxh, axis=-1, keepdims=True) - xh * jnp.mean(dxh * xh, axis=-1, keepdims=True))
        _accumulate(dg_ref, jnp.sum(dz * xh, axis=0, keepdims=True))
        _accumulate(db_ref, jnp.sum(dz, axis=0, keepdims=True))

    return pl.pallas_call(
        body, name="conv_ln_silu_bwd", grid=(S // ROW_TILE,),
        in_specs=[_row_spec(C), _vec_spec(C), _vec_spec(C), _row_spec(C)],
        out_specs=[_row_spec(C), _vec_spec(C), _vec_spec(C)],
        out_shape=[jax.ShapeDtypeStruct((S, C), F32), jax.ShapeDtypeStruct((1, C), F32), jax.ShapeDtypeStruct((1, C), F32)],
        compiler_params=_params(("arbitrary",)),
    )(c1, g, b, dc)


def _mix_call(proj, gate_col0, b_gate, y_a, y_c):
    S, D = y_a.shape
    w = GATE_COLS
    nc = D // w
    ga0, gc0 = gate_col0 // w, (gate_col0 + D) // w

    def body(ga_ref, gc_ref, ba_ref, bc_ref, ya_ref, yc_ref, o_ref):
        o_ref[...] = (_sigmoid(ga_ref[...] + ba_ref[...]) * ya_ref[...]
                      + _sigmoid(gc_ref[...] + bc_ref[...]) * yc_ref[...]).astype(BF16)

    tile = lambda off: pl.BlockSpec((GATE_ROWS, w), lambda i, j: (i, off + j))
    vec = lambda off: pl.BlockSpec((1, w), lambda i, j: (0, off + j))
    return pl.pallas_call(
        body, name="gate_mix", grid=(S // GATE_ROWS, nc),
        in_specs=[tile(ga0), tile(gc0), vec(0), vec(nc), tile(0), tile(0)],
        out_specs=tile(0), out_shape=jax.ShapeDtypeStruct((S, D), BF16),
        compiler_params=_params(("parallel", "parallel")),
    )(proj, proj, b_gate, b_gate, y_a, y_c)


def _mix_bwd_call(dmixed, proj, gate_col0, b_gate, y_a, y_c):
    S, D = y_a.shape
    w = GATE_COLS
    nc = D // w
    ga0, gc0 = gate_col0 // w, (gate_col0 + D) // w

    def body(dm_ref, ga_ref, gc_ref, ba_ref, bc_ref, ya_ref, yc_ref, dya_ref, dyc_ref, dga_ref, dgc_ref, dba_ref, dbc_ref):
        dm = dm_ref[...]
        sa = _sigmoid(ga_ref[...] + ba_ref[...])
        sc = _sigmoid(gc_ref[...] + bc_ref[...])
        dya_ref[...] = (dm * sa).astype(BF16)
        dyc_ref[...] = (dm * sc).astype(BF16)
        dga = dm * ya_ref[...] * sa * (1.0 - sa)
        dgc = dm * yc_ref[...] * sc * (1.0 - sc)
        dga_ref[...] = dga.astype(BF16)
        dgc_ref[...] = dgc.astype(BF16)
        pa = jnp.sum(dga, axis=0, keepdims=True)
        pc = jnp.sum(dgc, axis=0, keepdims=True)

        @pl.when(pl.program_id(1) == 0)
        def _():
            dba_ref[...] = pa
            dbc_ref[...] = pc

        @pl.when(pl.program_id(1) > 0)
        def _():
            dba_ref[...] += pa
            dbc_ref[...] += pc

    tile = lambda off: pl.BlockSpec((GATE_ROWS, w), lambda j, i: (i, off + j))
    vec = lambda off: pl.BlockSpec((1, w), lambda j, i: (0, off + j))
    return pl.pallas_call(
        body, name="gate_mix_bwd", grid=(nc, S // GATE_ROWS),
        in_specs=[tile(0), tile(ga0), tile(gc0), vec(0), vec(nc), tile(0), tile(0)],
        out_specs=[tile(0), tile(0), tile(0), tile(0), vec(0), vec(0)],
        out_shape=[jax.ShapeDtypeStruct((S, D), BF16)] * 4 + [
                   jax.ShapeDtypeStruct((1, D), F32), jax.ShapeDtypeStruct((1, D), F32)],
        compiler_params=_params(("parallel", "arbitrary")),
    )(dmixed, proj, proj, b_gate, b_gate, y_a, y_c)


def _res1_call(x, out, g_post, g_pre):
    S, D = x.shape

    def body(x_ref, o_ref, gp_ref, gq_ref, x1_ref, h2_ref):
        x1 = x_ref[...] + _rms(o_ref[...], gp_ref[...])
        x1_ref[...] = x1
        h2_ref[...] = _rms(x1, gq_ref[...]).astype(BF16)

    return pl.pallas_call(
        body, name="residual_mix", grid=(S // ROW_TILE,),
        in_specs=[_row_spec(D), _row_spec(D), _vec_spec(D), _vec_spec(D)],
        out_specs=[_row_spec(D), _row_spec(D)],
        out_shape=[jax.ShapeDtypeStruct((S, D), F32), jax.ShapeDtypeStruct((S, D), BF16)],
        compiler_params=_params(("parallel",)),
    )(x, out, g_post, g_pre)


def _loss_call(y, x1, g_post, target):
    S, D = y.shape

    def body(y_ref, x1_ref, g_ref, t_ref, loss_ref, dx_ref, dy_ref, dg_ref):
        yv, gv = y_ref[...], g_ref[...]
        err = x1_ref[...] + _rms(yv, gv) - t_ref[...]
        dx2 = err * (1.0 / D)
        dx_ref[...] = dx2
        dy, dg = _rms_bwd(yv, gv, dx2)
        dy_ref[...] = dy.astype(BF16)
        _accumulate(dg_ref, dg)
        part = 0.5 * jnp.sum(jnp.mean(err * err, axis=-1, keepdims=True), axis=0, keepdims=True)
        _accumulate(loss_ref, jnp.broadcast_to(part, (SUBLANES, LANES)))

    return pl.pallas_call(
        body, name="residual_ffn_loss", grid=(S // ROW_TILE,),
        in_specs=[_row_spec(D), _row_spec(D), _vec_spec(D), _row_spec(D)],
        out_specs=[pl.BlockSpec((SUBLANES, LANES), lambda i: (0, 0)), _row_spec(D), _row_spec(D), _vec_spec(D)],
        out_shape=[jax.ShapeDtypeStruct((SUBLANES, LANES), F32), jax.ShapeDtypeStruct((S, D), F32),
                   jax.ShapeDtypeStruct((S, D), BF16), jax.ShapeDtypeStruct((1, D), F32)],
        compiler_params=_params(("arbitrary",)),
    )(y, x1, g_post, target)


def _mid_bwd_call(x1, g_pre, dh2, dx2, out, g_post):
    S, D = x1.shape

    def body(x1_ref, gq_ref, dh_ref, dx2_ref, o_ref, gp_ref, dx1_ref, do_ref, dgq_ref, dgp_ref):
        d, dgq = _rms_bwd(x1_ref[...], gq_ref[...], dh_ref[...])
        dx1 = dx2_ref[...] + d
        dx1_ref[...] = dx1
        do, dgp = _rms_bwd(o_ref[...], gp_ref[...], dx1)
        do_ref[...] = do.astype(BF16)
        _accumulate(dgq_ref, dgq)
        _accumulate(dgp_ref, dgp)

    return pl.pallas_call(
        body, name="residual_mix_bwd", grid=(S // ROW_TILE,),
        in_specs=[_row_spec(D), _vec_spec(D), _row_spec(D), _row_spec(D), _row_spec(D), _vec_spec(D)],
        out_specs=[_row_spec(D), _row_spec(D), _vec_spec(D), _vec_spec(D)],
        out_shape=[jax.ShapeDtypeStruct((S, D), F32), jax.ShapeDtypeStruct((S, D), BF16)] + [jax.ShapeDtypeStruct((1, D), F32)] * 2,
        compiler_params=_params(("arbitrary",)),
    )(x1, g_pre, dh2, dx2, out, g_post)


def _in_bwd_call(x, g, dh1, dx1):
    S, D = x.shape

    def body(x_ref, g_ref, dh_ref, dx1_ref, gx_ref, dg_ref):
        d, dg = _rms_bwd(x_ref[...], g_ref[...], dh_ref[...])
        gx_ref[...] = dx1_ref[...] + d
        _accumulate(dg_ref, dg)

    return pl.pallas_call(
        body, name="rms_mix_pre_bwd", grid=(S // ROW_TILE,),
        in_specs=[_row_spec(D), _vec_spec(D), _row_spec(D), _row_spec(D)],
        out_specs=[_row_spec(D), _vec_spec(D)],
        out_shape=[jax.ShapeDtypeStruct((S, D), F32), jax.ShapeDtypeStruct((1, D), F32)],
        compiler_params=_params(("arbitrary",)),
    )(x, g, dh1, dx1)


def _bucket_table(dilation):
    qi = np.arange(SPAN)[:, None]
    ki = np.arange(2 * SPAN)[None, :]
    dist = np.maximum(qi + SPAN - ki, 0) * dilation
    max_exact = N_BUCKETS // 2
    d = np.maximum(dist, 1).astype(np.float64)
    large = max_exact + (np.log(d / max_exact) / math.log(MAX_DISTANCE / max_exact) * (N_BUCKETS - max_exact)).astype(np.int32)
    large = np.minimum(large, N_BUCKETS - 1)
    return np.where(dist < max_exact, dist, large).astype(np.int32)


def _bucket_tables():
    return jnp.asarray(np.stack([_bucket_table(r) for _, r in DILATED_PATTERNS]))


def _bias_table_call(rel_bias, buckets):
    def body(rb_ref, bk_ref, o_ref):
        for h in range(N_HEADS):
            bk = bk_ref[h // HEADS_PER_GROUP]

            def step(b, acc):
                return jnp.where(bk == b, rb_ref[b, h], acc)

            o_ref[h] = lax.fori_loop(0, N_BUCKETS, step, jnp.zeros((SPAN, 2 * SPAN), F32))

    return pl.pallas_call(
        body, name="rel_bias_table",
        in_specs=[pl.BlockSpec(memory_space=pltpu.SMEM), pl.BlockSpec(memory_space=pltpu.VMEM)],
        out_specs=pl.BlockSpec(memory_space=pltpu.VMEM),
        out_shape=jax.ShapeDtypeStruct((N_HEADS, SPAN, 2 * SPAN), F32),
    )(rel_bias, buckets)


def _bias_grad_call(dbias, buckets):
    def body(db_ref, bk_ref, o_ref, rows_ref):
        for h in range(N_HEADS):
            bk = bk_ref[h // HEADS_PER_GROUP]
            dv = db_ref[h]

            def step(b, carry):
                rows_ref[h, b] = jnp.sum(jnp.where(bk == b, dv, 0.0), axis=0, keepdims=True)
                return carry

            lax.fori_loop(0, N_BUCKETS, step, 0)
        o_ref[...] = jnp.sum(rows_ref[...], axis=-1, keepdims=True)

    out = pl.pallas_call(
        body, name="rel_bias_grad",
        in_specs=[pl.BlockSpec(memory_space=pltpu.VMEM), pl.BlockSpec(memory_space=pltpu.VMEM)],
        out_specs=pl.BlockSpec(memory_space=pltpu.VMEM),
        out_shape=jax.ShapeDtypeStruct((N_HEADS, N_BUCKETS, 1, 1), F32),
        scratch_shapes=[pltpu.VMEM((N_HEADS, N_BUCKETS, 1, 2 * SPAN), F32)],
    )(dbias, buckets)
    return out.reshape(N_HEADS, N_BUCKETS).T


def _dot_nt(a, b):
    return lax.dot_general(a, b, (((1,), (1,)), ((), ())), preferred_element_type=F32)


def _dot_nn(a, b):
    return lax.dot_general(a, b, (((1,), (0,)), ((), ())), preferred_element_type=F32)


def _dot_tn(a, b):
    return lax.dot_general(a, b, (((0,), (0,)), ((), ())), preferred_element_type=F32)


def _band_masks(n, nb):
    qi = lax.broadcasted_iota(jnp.int32, (SPAN, SPAN), 0)
    ki = lax.broadcasted_iota(jnp.int32, (SPAN, SPAN), 1)
    prev_ok = jnp.logical_and(ki >= qi, n > 0)
    cur_ok = ki <= qi
    next_ok = jnp.logical_and(ki >= qi, n < nb - 1)
    return prev_ok, cur_ok, next_ok


def _wide_band_mask(n):
    qi = lax.broadcasted_iota(jnp.int32, (SPAN, 2 * SPAN), 0)
    ki = lax.broadcasted_iota(jnp.int32, (SPAN, 2 * SPAN), 1)
    prev_ok = jnp.logical_and(jnp.logical_and(ki < SPAN, ki >= qi), n > 0)
    cur_ok = jnp.logical_and(ki >= SPAN, ki - SPAN <= qi)
    return jnp.logical_or(prev_ok, cur_ok)


def _attn_plan(S, group):
    r = DILATED_PATTERNS[group][1]
    hp, per = (HEADS_PER_GROUP, 1) if r == 1 else (2, 4)
    return r, S // (r * SPAN), hp, per


def _residue_rows(rho, r):
    return slice(None) if r == 1 else pl.ds(rho, SPAN, stride=r)


def _for_residues(r, per, fn):
    if r == per:
        for u in range(per):
            fn(u)
        return

    def step(i, carry):
        for u in range(per):
            fn(i * per + u)
        return carry

    lax.fori_loop(0, r // per, step, 0)


def _attn_fwd_call(proj, bias, group):
    S = proj.shape[0]
    r, nb, hp, per = _attn_plan(S, group)
    scale = HEAD_DIM ** -0.5
    kinds = ("q", "kp", "kc", "vp", "vc") if nb > 1 else ("q", "kc", "vc")

    def body(*refs):
        ins = {kind: refs[i * hp:(i + 1) * hp] for i, kind in enumerate(kinds)}
        b_ref, o_ref, lse_ref = refs[len(kinds) * hp:]
        n = pl.program_id(1)
        prev_ok, cur_ok, _ = _band_masks(n, nb)

        band_ok = _wide_band_mask(n) if nb > 1 else cur_ok

        def residue(rho):
            rows = _residue_rows(rho, r)
            for j in range(hp):
                get = lambda kind: ins[kind][j][rows, :].astype(BF16)
                q = get("q")
                if nb > 1:
                    keys, vals, bias_j = jnp.concatenate([get("kp"), get("kc")], axis=0), jnp.concatenate([get("vp"), get("vc")], axis=0), b_ref[j]
                else:
                    keys, vals, bias_j = get("kc"), get("vc"), b_ref[j, :, SPAN:]
                s = jnp.where(band_ok, _dot_nt(q, keys) * scale + bias_j, NEG_INF)
                m = jnp.max(s, axis=-1, keepdims=True)
                p = jnp.exp(s - m)
                den = jnp.sum(p, axis=-1, keepdims=True)
                o_ref[j, rows, :] = _dot_nn(p.astype(BF16), vals) / den
                lse_ref[j, rows, :] = jnp.broadcast_to(m + jnp.log(den), (SPAN, HEAD_DIM))

        _for_residues(r, per, residue)

    in_specs = [_head_spec(r, nb, hp, kind, group, jj) for kind in kinds for jj in range(hp)]
    in_specs.append(pl.BlockSpec((hp, SPAN, 2 * SPAN), lambda j, n: (group * (HEADS_PER_GROUP // hp) + j, 0, 0)))
    out = pl.BlockSpec((hp, r * SPAN, HEAD_DIM), lambda j, n: (j, n, 0))
    return pl.pallas_call(
        body, name=f"attn_fwd_g{group}", grid=(HEADS_PER_GROUP // hp, nb),
        in_specs=in_specs, out_specs=[out] * 2,
        out_shape=[jax.ShapeDtypeStruct((HEADS_PER_GROUP, S, HEAD_DIM), F32)] * 2,
        compiler_params=_params(("parallel", "parallel"), VMEM_LIMIT),
    )(*([proj] * (len(in_specs) - 1)), bias)


_PROJ_PART = dict(q=0, qn=0, kp=1, kc=1, vp=2, vc=2)


def _head_spec(r, nb, hp, kind, group, jj):
    if kind in _PROJ_PART:
        base = (_PROJ_PART[kind] * N_GROUPS + group) * HEADS_PER_GROUP
    else:
        base = 0
    if kind.endswith("p"):
        row = lambda n: jnp.maximum(n - 1, 0)
    elif kind.endswith("n"):
        row = lambda n: jnp.minimum(n + 1, nb - 1)
    else:
        row = lambda n: n
    return pl.BlockSpec((r * SPAN, HEAD_DIM), lambda j, n: (row(n), base + j * hp + jj))


def _attn_merge_call(parts):
    S = parts[0].shape[1]

    def body(o1, s1, o2, s2, o3, s3, a_ref, ab_ref, lse_ref):
        for j in range(HEADS_PER_GROUP):
            sl = slice(j * HEAD_DIM, (j + 1) * HEAD_DIM)
            mx = jnp.maximum(jnp.maximum(s1[j], s2[j]), s3[j])
            w1 = jnp.exp(s1[j] - mx)
            w2 = jnp.exp(s2[j] - mx)
            w3 = jnp.exp(s3[j] - mx)
            den = w1 + w2 + w3
            a = (w1 * o1[j] + w2 * o2[j] + w3 * o3[j]) / den
            a_ref[:, sl] = a
            ab_ref[:, sl] = a.astype(BF16)
            lse_ref[:, sl] = mx + jnp.log(den)

    heads = pl.BlockSpec((HEADS_PER_GROUP, ROW_TILE, HEAD_DIM), lambda i: (0, i, 0))
    return pl.pallas_call(
        body, name="attn_merge", grid=(S // ROW_TILE,),
        in_specs=[heads] * 6, out_specs=[_row_spec(GROUP_WIDTH)] * 3,
        out_shape=[jax.ShapeDtypeStruct((S, GROUP_WIDTH), F32), jax.ShapeDtypeStruct((S, GROUP_WIDTH), BF16),
                   jax.ShapeDtypeStruct((S, GROUP_WIDTH), F32)],
        compiler_params=_params(("parallel",)),
    )(*parts)


def _attn_delta_call(a, da):
    S = a.shape[0]

    def body(a_ref, da_ref, d_ref):
        for j in range(HEADS_PER_GROUP):
            sl = slice(j * HEAD_DIM, (j + 1) * HEAD_DIM)
            d = jnp.sum(a_ref[:, sl] * da_ref[:, sl], axis=-1, keepdims=True)
            d_ref[:, sl] = jnp.broadcast_to(d, (ROW_TILE, HEAD_DIM))

    return pl.pallas_call(
        body, name="attn_delta", grid=(S // ROW_TILE,),
        in_specs=[_row_spec(GROUP_WIDTH)] * 2, out_specs=_row_spec(GROUP_WIDTH),
        out_shape=jax.ShapeDtypeStruct((S, GROUP_WIDTH), F32),
        compiler_params=_params(("parallel",)),
    )(a, da)


def _attn_bwd_call(proj, bias, da, lse, delta, group):
    S = proj.shape[0]
    r, nb, hp, per = _attn_plan(S, group)
    scale = HEAD_DIM ** -0.5
    kinds = ("q", "qn", "kp", "kc", "vp", "vc", "da", "dan", "lse", "lsen", "dl", "dln") if nb > 1 else ("q", "kc", "vc", "da", "lse", "dl")
    source = dict(da=da, dan=da, lse=lse, lsen=lse, dl=delta, dln=delta)

    def body(*refs):
        ins = {kind: refs[i * hp:(i + 1) * hp] for i, kind in enumerate(kinds)}
        b_ref, dq_ref, dk_ref, dv_ref, db_ref = refs[len(kinds) * hp:]
        n = pl.program_id(1)
        prev_ok, cur_ok, next_ok = _band_masks(n, nb)

        @pl.when(n == 0)
        def _():
            db_ref[...] = jnp.zeros_like(db_ref)

        band_ok = _wide_band_mask(n) if nb > 1 else cur_ok

        def residue(rho):
            rows = _residue_rows(rho, r)
            for j in range(hp):
                get = lambda kind: ins[kind][j][rows, :]
                q = get("q").astype(BF16)
                kc = get("kc").astype(BF16)
                vc = get("vc").astype(BF16)
                dav = get("da").astype(BF16)
                lse_q, dl_q = get("lse"), get("dl")
                if nb == 1:
                    pc = jnp.exp(jnp.where(cur_ok, _dot_nt(q, kc) * scale + b_ref[j, :, SPAN:], NEG_INF) - lse_q)
                    dsc = pc * (_dot_nt(dav, vc) - dl_q)
                    dsc_b = dsc.astype(BF16)
                    dq = _dot_nn(dsc_b, kc)
                    dk = _dot_tn(dsc_b, q)
                    dv = _dot_tn(pc.astype(BF16), dav)
                    db_ref[j, :, SPAN:] += dsc
                else:
                    qn = get("qn").astype(BF16)
                    dan = get("dan").astype(BF16)
                    keys = jnp.concatenate([get("kp").astype(BF16), kc], axis=0)
                    vals = jnp.concatenate([get("vp").astype(BF16), vc], axis=0)
                    wide = lambda t: jnp.concatenate([t, t], axis=1)
                    p = jnp.exp(jnp.where(band_ok, _dot_nt(q, keys) * scale + b_ref[j], NEG_INF) - wide(lse_q))
                    ds = p * (_dot_nt(dav, vals) - wide(dl_q))
                    dq = _dot_nn(ds.astype(BF16), keys)
                    db_ref[j] += ds
                    pn = jnp.exp(jnp.where(next_ok, _dot_nt(qn, kc) * scale + b_ref[j, :, :SPAN], NEG_INF) - get("lsen"))
                    dsn = pn * (_dot_nt(dan, vc) - get("dln"))
                    both = lambda cur_part, next_part: jnp.concatenate([cur_part.astype(BF16), next_part.astype(BF16)], axis=0)
                    dk = _dot_tn(both(ds[:, SPAN:], dsn), jnp.concatenate([q, qn], axis=0))
                    dv = _dot_tn(both(p[:, SPAN:], pn), jnp.concatenate([dav, dan], axis=0))
                dq_ref[j, rows, :] = dq * scale
                dk_ref[j, rows, :] = dk * scale
                dv_ref[j, rows, :] = dv

        _for_residues(r, per, residue)

    per_group = HEADS_PER_GROUP // hp
    band = (hp, SPAN, 2 * SPAN)
    in_specs = [_head_spec(r, nb, hp, kind, group, jj) for kind in kinds for jj in range(hp)]
    in_specs.append(pl.BlockSpec(band, lambda j, n: (group * per_group + j, 0, 0)))
    operands = [source.get(kind, proj) for kind in kinds for _ in range(hp)] + [bias]
    out = pl.BlockSpec((hp, r * SPAN, HEAD_DIM), lambda j, n: (j, n, 0))
    return pl.pallas_call(
        body, name=f"attn_bwd_g{group}", grid=(per_group, nb),
        in_specs=in_specs,
        out_specs=[out] * 3 + [pl.BlockSpec(band, lambda j, n: (j, 0, 0))],
        out_shape=[jax.ShapeDtypeStruct((HEADS_PER_GROUP, S, HEAD_DIM), F32)] * 3
        + [jax.ShapeDtypeStruct((HEADS_PER_GROUP, SPAN, 2 * SPAN), F32)],
        compiler_params=_params(("parallel", "arbitrary"), VMEM_LIMIT),
    )(*operands)


def _dproj_call(dqkv, tails):
    S = tails[0].shape[0]
    width = len(dqkv) * GROUP_WIDTH + sum(t.shape[1] for t in tails)

    def body(*refs):
        o_ref = refs[-1]
        col = 0
        for ref in refs[:len(dqkv)]:
            for j in range(HEADS_PER_GROUP):
                o_ref[:, col:col + HEAD_DIM] = ref[j].astype(BF16)
                col += HEAD_DIM
        for ref in refs[len(dqkv):-1]:
            o_ref[:, col:col + ref.shape[1]] = ref[...]
            col += ref.shape[1]

    heads = pl.BlockSpec((HEADS_PER_GROUP, ROW_TILE, HEAD_DIM), lambda i: (0, i, 0))
    return pl.pallas_call(
        body, name="dproj_assemble", grid=(S // ROW_TILE,),
        in_specs=[heads] * len(dqkv) + [_row_spec(t.shape[1]) for t in tails],
        out_specs=_row_spec(width), out_shape=jax.ShapeDtypeStruct((S, width), BF16),
        compiler_params=_params(("parallel",)),
    )(*dqkv, *tails)


def _tap_rows(xpad_ref, t0, k, width, pad):
    return xpad_ref[pl.ds(t0 + (pad - (width - 1 - k)), TIME_BLOCK), :]


def _conv_block(xpad_ref, t0, w_ref, width, pad):
    acc = None
    for k in range(width):
        term = w_ref[k:k + 1, :] * _tap_rows(xpad_ref, t0, k, width, pad)
        acc = term if acc is None else acc + term
    return acc


def _conv_transpose_block(dpad_ref, t0, w_ref, width):
    acc = None
    for k in range(width):
        term = w_ref[k:k + 1, :] * dpad_ref[pl.ds(t0 + (width - 1 - k), TIME_BLOCK), :]
        acc = term if acc is None else acc + term
    return acc


def _conv_weight_grad(xpad_ref, t0, dy, dw_ref, width, pad):
    for k in range(width):
        dw_ref[k:k + 1, :] += jnp.sum(dy * _tap_rows(xpad_ref, t0, k, width, pad), axis=0, keepdims=True)


def _time_loop(S, step):
    def it(tb, carry):
        step(pl.multiple_of(tb * TIME_BLOCK, TIME_BLOCK))
        return carry

    lax.fori_loop(0, S // TIME_BLOCK, it, 0)


def _conv_fwd_call(proj, col0, w, b):
    S = proj.shape[0]
    C = w.shape[1]
    nt = C // LANES
    v0, g0 = col0 // LANES, (col0 + C) // LANES

    def body(val_ref, gate_ref, w_ref, b_ref, o_ref, pad_ref):
        pad_ref[0:CONV_PAD, :] = jnp.zeros((CONV_PAD, LANES), F32)
        pad_ref[CONV_PAD:, :] = val_ref[...] * _sigmoid(gate_ref[...])

        def step(t0):
            o_ref[pl.ds(t0, TIME_BLOCK), :] = _conv_block(pad_ref, t0, w_ref, CONV_WIDTH, CONV_PAD) + b_ref[...]

        _time_loop(S, step)

    seq = lambda off: pl.BlockSpec((S, LANES), lambda i: (0, off + i))
    return pl.pallas_call(
        body, name="conv_module", grid=(nt,),
        in_specs=[seq(v0), seq(g0), pl.BlockSpec((CONV_WIDTH, LANES), lambda i: (0, i)), pl.BlockSpec((1, LANES), lambda i: (0, i))],
        out_specs=seq(0), out_shape=jax.ShapeDtypeStruct((S, C), F32),
        scratch_shapes=[pltpu.VMEM((S + CONV_PAD, LANES), F32)],
        compiler_params=_params(("parallel",)),
    )(proj, proj, w, b)


def _conv_bwd_call(proj, col0, w, dc1):
    S = proj.shape[0]
    C = w.shape[1]
    nt = C // LANES
    v0, g0 = col0 // LANES, (col0 + C) // LANES

    def body(val_ref, gate_ref, w_ref, dy_ref, dval_ref, dgate_ref, dw_ref, db_ref, xpad_ref, dpad_ref, dwacc_ref):
        xpad_ref[0:CONV_PAD, :] = jnp.zeros((CONV_PAD, LANES), F32)
        xpad_ref[CONV_PAD:, :] = val_ref[...] * _sigmoid(gate_ref[...])
        dpad_ref[0:S, :] = dy_ref[...]
        dpad_ref[S:, :] = jnp.zeros((CONV_PAD, LANES), F32)
        dwacc_ref[...] = jnp.zeros_like(dwacc_ref)

        def step(t0):
            rows = pl.ds(t0, TIME_BLOCK)
            _conv_weight_grad(xpad_ref, t0, dy_ref[rows, :], dwacc_ref, CONV_WIDTH, CONV_PAD)
            dc0 = _conv_transpose_block(dpad_ref, t0, w_ref, CONV_WIDTH)
            sg = _sigmoid(gate_ref[rows, :])
            dval_ref[rows, :] = (dc0 * sg).astype(BF16)
            dgate_ref[rows, :] = (dc0 * val_ref[rows, :] * sg * (1.0 - sg)).astype(BF16)

        _time_loop(S, step)
        dw_ref[...] = dwacc_ref[...]
        db_ref[...] = jnp.sum(dy_ref[...], axis=0, keepdims=True)

    seq = lambda off: pl.BlockSpec((S, LANES), lambda i: (0, off + i))
    return pl.pallas_call(
        body, name="conv_module_bwd", grid=(nt,),
        in_specs=[seq(v0), seq(g0), pl.BlockSpec((CONV_WIDTH, LANES), lambda i: (0, i)), seq(0)],
        out_specs=[seq(0), seq(0), pl.BlockSpec((CONV_PAD, LANES), lambda i: (0, i)), pl.BlockSpec((1, LANES), lambda i: (0, i))],
        out_shape=[jax.ShapeDtypeStruct((S, C), BF16), jax.ShapeDtypeStruct((S, C), BF16),
                   jax.ShapeDtypeStruct((CONV_PAD, C), F32), jax.ShapeDtypeStruct((1, C), F32)],
        scratch_shapes=[pltpu.VMEM((S + CONV_PAD, LANES), F32), pltpu.VMEM((S + CONV_PAD, LANES), F32),
                        pltpu.VMEM((CONV_PAD, LANES), F32)],
        compiler_params=_params(("parallel",)),
    )(proj, proj, w, dc1)


def _ffn_fwd_call(u, w, b):
    S, C2 = u.shape
    C = C2 // 2
    nt = C // LANES

    def body(ug_ref, uv_ref, wg_ref, wv_ref, bg_ref, bv_ref, f_ref, pg_ref, pv_ref):
        zeros = jnp.zeros((FFN_PAD, LANES), F32)
        pg_ref[0:FFN_PAD, :] = zeros
        pv_ref[0:FFN_PAD, :] = zeros
        pg_ref[FFN_PAD:, :] = ug_ref[...]
        pv_ref[FFN_PAD:, :] = uv_ref[...]

        def step(t0):
            cg = _conv_block(pg_ref, t0, wg_ref, FFN_CONV_WIDTH, FFN_PAD) + bg_ref[...]
            cv = _conv_block(pv_ref, t0, wv_ref, FFN_CONV_WIDTH, FFN_PAD) + bv_ref[...]
            f_ref[pl.ds(t0, TIME_BLOCK), :] = (_gelu(cg) * cv).astype(BF16)

        _time_loop(S, step)

    seq = lambda off: pl.BlockSpec((S, LANES), lambda i: (0, off + i))
    wsp = lambda off: pl.BlockSpec((FFN_CONV_WIDTH, LANES), lambda i: (0, off + i))
    bsp = lambda off: pl.BlockSpec((1, LANES), lambda i: (0, off + i))
    return pl.pallas_call(
        body, name="ffn_conv_geglu", grid=(nt,),
        in_specs=[seq(0), seq(nt), wsp(0), wsp(nt), bsp(0), bsp(nt)],
        out_specs=seq(0), out_shape=jax.ShapeDtypeStruct((S, C), BF16),
        scratch_shapes=[pltpu.VMEM((S + FFN_PAD, LANES), F32)] * 2,
        compiler_params=_params(("parallel",)),
    )(u, u, w, w, b, b)


def _ffn_bwd_call(u, w, b, df):
    S, C2 = u.shape
    C = C2 // 2
    nt = C // LANES

    def body(ug_ref, uv_ref, wg_ref, wv_ref, bg_ref, bv_ref, df_ref,
             du_ref, dwg_ref, dwv_ref, dbg_ref, dbv_ref,
             pg_ref, pv_ref, dg_ref, dv_ref, dwg_acc, dwv_acc, dbg_acc, dbv_acc):
        zeros = jnp.zeros((FFN_PAD, LANES), F32)
        pg_ref[0:FFN_PAD, :] = zeros
        pv_ref[0:FFN_PAD, :] = zeros
        pg_ref[FFN_PAD:, :] = ug_ref[...]
        pv_ref[FFN_PAD:, :] = uv_ref[...]
        dg_ref[S:, :] = zeros
        dv_ref[S:, :] = zeros
        dwg_acc[...] = jnp.zeros_like(dwg_acc)
        dwv_acc[...] = jnp.zeros_like(dwv_acc)
        dbg_acc[...] = jnp.zeros_like(dbg_acc)
        dbv_acc[...] = jnp.zeros_like(dbv_acc)

        def first(t0):
            rows = pl.ds(t0, TIME_BLOCK)
            cg = _conv_block(pg_ref, t0, wg_ref, FFN_CONV_WIDTH, FFN_PAD) + bg_ref[...]
            cv = _conv_block(pv_ref, t0, wv_ref, FFN_CONV_WIDTH, FFN_PAD) + bv_ref[...]
            dfb = df_ref[rows, :]
            gelu, gelu_grad = _gelu_and_grad(cg)
            dcg = dfb * cv * gelu_grad
            dcv = dfb * gelu
            dg_ref[rows, :] = dcg
            dv_ref[rows, :] = dcv
            _conv_weight_grad(pg_ref, t0, dcg, dwg_acc, FFN_CONV_WIDTH, FFN_PAD)
            _conv_weight_grad(pv_ref, t0, dcv, dwv_acc, FFN_CONV_WIDTH, FFN_PAD)
            dbg_acc[...] += jnp.sum(dcg, axis=0, keepdims=True)
            dbv_acc[...] += jnp.sum(dcv, axis=0, keepdims=True)

        def second(t0):
            rows = pl.ds(t0, TIME_BLOCK)
            du_ref[0, rows, :] = _conv_transpose_block(dg_ref, t0, wg_ref, FFN_CONV_WIDTH).astype(BF16)
            du_ref[1, rows, :] = _conv_transpose_block(dv_ref, t0, wv_ref, FFN_CONV_WIDTH).astype(BF16)

        _time_loop(S, first)
        _time_loop(S, second)
        dwg_ref[...] = dwg_acc[...]
        dwv_ref[...] = dwv_acc[...]
        dbg_ref[...] = dbg_acc[...]
        dbv_ref[...] = dbv_acc[...]

    seq = lambda off: pl.BlockSpec((S, LANES), lambda i: (0, off + i))
    wsp = lambda off: pl.BlockSpec((FFN_CONV_WIDTH, LANES), lambda i: (0, off + i))
    bsp = lambda off: pl.BlockSpec((1, LANES), lambda i: (0, off + i))
    return pl.pallas_call(
        body, name="ffn_conv_geglu_bwd", grid=(nt,),
        in_specs=[seq(0), seq(nt), wsp(0), wsp(nt), bsp(0), bsp(nt), seq(0)],
        out_specs=[pl.BlockSpec((2, S, LANES), lambda i: (0, 0, i)),
                   pl.BlockSpec((SUBLANES, LANES), lambda i: (0, i)), pl.BlockSpec((SUBLANES, LANES), lambda i: (0, i)),
                   bsp(0), bsp(0)],
        out_shape=[jax.ShapeDtypeStruct((2, S, C), BF16)] + [jax.ShapeDtypeStruct((SUBLANES, C), F32)] * 2
        + [jax.ShapeDtypeStruct((1, C), F32)] * 2,
        scratch_shapes=[pltpu.VMEM((S + FFN_PAD, LANES), F32)] * 4 + [pltpu.VMEM((SUBLANES, LANES), F32)] * 2
        + [pltpu.VMEM((1, LANES), F32)] * 2,
        compiler_params=_params(("parallel",)),
    )(u, u, w, w, b, b, df)


def _adamw(w_ref, g_ref, m_ref, v_ref, d_ref, mo_ref, vo_ref):
    gv = g_ref[...]
    mn = ADAM_B1 * m_ref[...] + (1.0 - ADAM_B1) * gv
    vn = ADAM_B2 * v_ref[...] + (1.0 - ADAM_B2) * (gv * gv)
    mo_ref[...] = mn
    vo_ref[...] = vn
    m_hat = mn * (1.0 / (1.0 - ADAM_B1 ** ADAM_STEP))
    v_hat = vn * (1.0 / (1.0 - ADAM_B2 ** ADAM_STEP))
    d_ref[...] = -ADAM_LR * (m_hat / (jnp.sqrt(v_hat) + ADAM_EPS) + ADAM_WD * w_ref[...])


def _adamw_call(w, g, m, v, name):
    R, C = w.shape
    tr = _row_tile(R, C)

    def body(w_ref, g_ref, m_ref, v_ref, go_ref, d_ref, mo_ref, vo_ref):
        go_ref[...] = g_ref[...]
        _adamw(w_ref, g_ref, m_ref, v_ref, d_ref, mo_ref, vo_ref)

    spec = pl.BlockSpec((tr, C), lambda i: (i, 0))
    return pl.pallas_call(
        body, name=name, grid=(R // tr,),
        in_specs=[spec] * 4, out_specs=[spec] * 4,
        out_shape=[jax.ShapeDtypeStruct((R, C), F32)] * 4,
        compiler_params=_params(("parallel",)),
    )(w, g, m, v)


def _adamw_small_call(ws, gs, ms, vs):
    n = len(ws)

    def body(*refs):
        w_refs, g_refs, m_refs, v_refs, d_refs, mo_refs, vo_refs = (refs[i * n:(i + 1) * n] for i in range(7))
        for i in range(n):
            _adamw(w_refs[i], g_refs[i], m_refs[i], v_refs[i], d_refs[i], mo_refs[i], vo_refs[i])

    whole = pl.BlockSpec(memory_space=pltpu.VMEM)
    outs = pl.pallas_call(
        body, name="adamw_small",
        in_specs=[whole] * (4 * n), out_specs=[whole] * (3 * n),
        out_shape=[jax.ShapeDtypeStruct(w.shape, F32) for w in ws] * 3,
    )(*ws, *gs, *ms, *vs)
    return outs[:n], outs[n:2 * n], outs[2 * n:]


def _position():
    return lax.axis_index("x"), lax.axis_index("y"), lax.axis_index("c")


def _chip_peers(x, y):
    return [(x, 1 - y), (1 - x, y), (1 - x, 1 - y)]


def _half_rows(ref, core, rows):
    h = rows // 2
    start = pl.multiple_of(core * h, 16)
    return ref.at[pl.ds(start, h), :] if len(ref.shape) == 2 else ref.at[:, pl.ds(start, h), :]


def _shard_half(ref, shard, core, rows):
    h = rows // 2
    return ref.at[shard, pl.ds(pl.multiple_of(core * h, 16), h), :]


ANY = pl.BlockSpec(memory_space=pl.ANY)


def _first_hop_copies(srcs, lands):
    x, y, c = _position()
    chip = 2 * x + y
    targets = [(px, py, c) for px, py in _chip_peers(x, y)] + [(x, y, 1 - c)]
    rows = srcs[0].shape[0]
    out = []
    for i, (s, l) in enumerate(zip(srcs, lands)):
        for k, dev in enumerate(targets):
            if i == 0 and k < 3:
                out.append((_half_rows(s, c, rows), _shard_half(l, chip, c, rows), dev, k))
            else:
                out.append((s, l.at[chip], dev, len(targets) * i + k))
    return out


def _second_hop_copies(srcs, lands):
    x, y, c = _position()
    rows = lands[0].shape[1]
    out = []
    for k, (px, py) in enumerate(_chip_peers(x, y)):
        half = _shard_half(lands[0], 2 * px + py, c, rows)
        out.append((half, half, (x, y, 1 - c), k))
    return out


HBM_SPEC = pl.BlockSpec(memory_space=pltpu.HBM)
SEM_SPEC = pl.BlockSpec(memory_space=pltpu.SEMAPHORE)
DATAFLOW = pltpu.SideEffectType.DATAFLOW_SIDE_EFFECTING


def _in_hbm(a):
    return pltpu.with_memory_space_constraint(a, pltpu.HBM)


def _split_start(name, groups, after):
    spans, arrays = [], []
    for srcs, lands, _, _ in groups:
        spans.append((len(arrays), len(srcs), len(lands)))
        arrays += list(srcs) + list(lands)
    na, ng = len(arrays), len(groups)

    def body(*refs):
        sems, token = refs[na + 1:na + 1 + 2 * ng], refs[-1]
        for g, (_, _, _, copies) in enumerate(groups):
            off, ns, nl = spans[g]
            for src, dst, dev, idx in copies(refs[off:off + ns], refs[off + ns:off + ns + nl]):
                pltpu.make_async_remote_copy(src_ref=src, dst_ref=dst, send_sem=sems[2 * g].at[idx], recv_sem=sems[2 * g + 1].at[idx],
                                             device_id=dev, device_id_type=MESH).start()
        token[...] = jnp.zeros_like(token)

    outs = pl.pallas_call(
        body, name=name,
        in_specs=[HBM_SPEC] * na + [ANY],
        out_specs=[SEM_SPEC] * (2 * ng) + [HBM_SPEC] * na + [pl.BlockSpec(memory_space=pltpu.VMEM)],
        out_shape=[pltpu.SemaphoreType.DMA((n_sems,)) for _, _, n_sems, _ in groups for _ in range(2)]
        + [pltpu.HBM(a.shape, a.dtype) for a in arrays] + [jax.ShapeDtypeStruct((SUBLANES, LANES), F32)],
        input_output_aliases={i: 2 * ng + i for i in range(na)},
        compiler_params=pltpu.CompilerParams(has_side_effects=DATAFLOW),
    )(*[_in_hbm(a) for a in arrays], after)
    started = []
    for g, (off, ns, nl) in enumerate(spans):
        thru = outs[2 * ng + off:2 * ng + off + ns + nl]
        started.append(dict(send=outs[2 * g], recv=outs[2 * g + 1], srcs=list(thru[:ns]), lands=list(thru[ns:]),
                            tile=outs[-1], token=outs[-1][0, 0]))
    return started


def _split_wait(name, started, copies, after):
    n, m = len(started["srcs"]), len(started["lands"])

    def body(*refs):
        src_refs, land_refs = refs[:n], refs[n:n + m]
        send_sem, recv_sem = refs[n + m], refs[n + m + 1]
        for src, dst, dev, idx in copies(src_refs, land_refs):
            cp = pltpu.make_async_remote_copy(src_ref=src, dst_ref=dst, send_sem=send_sem.at[idx], recv_sem=recv_sem.at[idx],
                                              device_id=dev, device_id_type=MESH)
            cp.wait_send()
            cp.wait_recv()

    arrays = started["srcs"] + started["lands"]
    outs = pl.pallas_call(
        body, name=name,
        in_specs=[HBM_SPEC] * (n + m) + [SEM_SPEC, SEM_SPEC, ANY],
        out_specs=[HBM_SPEC] * (n + m),
        out_shape=[pltpu.HBM(a.shape, a.dtype) for a in arrays],
        input_output_aliases={i: i for i in range(n + m)},
        compiler_params=pltpu.CompilerParams(has_side_effects=DATAFLOW),
    )(*arrays, started["send"], started["recv"], after)
    return list(outs)


def _gather_copies(srcs, lands):
    x, y, c = _position()
    chip = 2 * x + y
    targets = [(px, py, c) for px, py in _chip_peers(x, y)] + [(x, y, 1 - c)]
    return [(s, l.at[chip], dev, len(targets) * i + k) for i, (s, l) in enumerate(zip(srcs, lands)) for k, dev in enumerate(targets)]


def _sibling_copies(srcs, lands):
    x, y, c = _position()
    return [(_half_rows(srcs[0], 1 - c, srcs[0].shape[1]), lands[0], (x, y, 1 - c), 0)]


def _exchange_copies(srcs, lands):
    x, y, c = _position()
    return [(srcs[0].at[2 * px + py], lands[0].at[k], (px, py, c), k) for k, (px, py) in enumerate(_chip_peers(x, y))]


def _pair_sum_call(grad, recv, core, name):
    _, h, B = recv.shape
    tr = _row_tile(h, B)

    def body(core_ref, g_ref, r_ref, o_ref, ob_ref):
        s = g_ref[...] + r_ref[...]
        o_ref[...] = s
        ob_ref[...] = s.astype(BF16)

    g_spec = pl.BlockSpec((None, tr, B), lambda q, i, core_ref: (q, core_ref[0] * (h // tr) + i, 0))
    spec = pl.BlockSpec((None, tr, B), lambda q, i, core_ref: (q, i, 0))
    return pl.pallas_call(
        body, name=name,
        grid_spec=pltpu.PrefetchScalarGridSpec(num_scalar_prefetch=1, grid=(N_CHIPS, h // tr), in_specs=[g_spec, spec],
                                               out_specs=[spec, spec]),
        out_shape=[jax.ShapeDtypeStruct(recv.shape, F32), jax.ShapeDtypeStruct(recv.shape, BF16)],
        compiler_params=_params(("parallel", "parallel")),
    )(core, grad, recv)


def _chip_sum_call(partial, recv, chip_core, name):
    _, h, B = recv.shape
    tr = _row_tile(h, B)

    def body(cc_ref, p_ref, r_ref, o_ref):
        o_ref[...] = ((p_ref[...] + r_ref[0].astype(F32)) + r_ref[1].astype(F32)) + r_ref[2].astype(F32)

    return pl.pallas_call(
        body, name=name,
        grid_spec=pltpu.PrefetchScalarGridSpec(
            num_scalar_prefetch=1, grid=(h // tr,),
            in_specs=[pl.BlockSpec((None, tr, B), lambda i, cc_ref: (cc_ref[0], i, 0)),
                      pl.BlockSpec((3, tr, B), lambda i, cc_ref: (0, i, 0))],
            out_specs=pl.BlockSpec((tr, B), lambda i, cc_ref: (cc_ref[1] * (h // tr) + i, 0))),
        out_shape=jax.ShapeDtypeStruct((2 * h, B), F32),
        compiler_params=_params(("parallel",)),
    )(chip_core, partial, recv)


def _sibling_assemble_call(shards, name="grad_sibling_assemble"):
    n = len(shards)

    def body(*refs):
        ins, outs = refs[:n], refs[n:2 * n]
        send_sems, recv_sems = refs[2 * n:]
        x, y, c = _position()
        copies = []
        for i in range(n):
            rows = shards[i].shape[0]
            cp = pltpu.make_async_remote_copy(src_ref=_half_rows(ins[i], c, rows), dst_ref=_half_rows(outs[i], c, rows),
                                              send_sem=send_sems.at[i], recv_sem=recv_sems.at[i],
                                              device_id=(x, y, 1 - c), device_id_type=MESH)
            cp.start()
            copies.append(cp)
        for cp in copies:
            cp.wait()

    return pl.pallas_call(
        body, name=name,
        in_specs=[ANY] * n, out_specs=[ANY] * n,
        out_shape=[jax.ShapeDtypeStruct(s.shape, F32) for s in shards],
        input_output_aliases={i: i for i in range(n)},
        scratch_shapes=[pltpu.SemaphoreType.DMA((n,)), pltpu.SemaphoreType.DMA((n,))],
    )(*shards)


N_DEVICES = 8


def _allsum_copies(srcs, lands):
    x, y, c = _position()
    me = 4 * x + 2 * y + c
    out = []
    for k in range(1, N_DEVICES):
        peer = (1 - x if k & 4 else x, 1 - y if k & 2 else y, 1 - c if k & 1 else c)
        out.append((srcs[0], lands[0].at[me], peer, k - 1))
    return out


def _ordered_sum_call(mine, landed, me):
    rows = mine.shape[0]

    def body(me_ref, x_ref, l_ref, o_ref):
        acc = jnp.where(me_ref[0] == 0, x_ref[...], l_ref[0])
        for d in range(1, N_DEVICES):
            acc = acc + jnp.where(me_ref[0] == d, x_ref[...], l_ref[d])
        o_ref[...] = acc

    return pl.pallas_call(
        body, name="small_grad_sum",
        in_specs=[pl.BlockSpec(memory_space=pltpu.SMEM), pl.BlockSpec(memory_space=pltpu.VMEM), pl.BlockSpec(memory_space=pltpu.VMEM)],
        out_specs=pl.BlockSpec(memory_space=pltpu.VMEM),
        out_shape=jax.ShapeDtypeStruct((rows, LANES), F32),
    )(me, mine, landed)


def _pack(arrays):
    flat = jnp.concatenate([a.reshape(-1).astype(F32) for a in arrays])
    rows = -(-flat.shape[0] // LANES)
    rows = -(-rows // SUBLANES) * SUBLANES
    flat = jnp.pad(flat, (0, rows * LANES - flat.shape[0]))
    return flat.reshape(rows, LANES)


def _unpack(packed, shapes):
    flat = packed.reshape(-1)
    out, off = [], 0
    for shp in shapes:
        size = int(np.prod(shp))
        out.append(flat[off:off + size].reshape(shp))
        off += size
    return out


def _local_step(xs, target, P, late_weights, on_grad):
    S, D = xs.shape
    qkv_width = 3 * N_HEADS * HEAD_DIM
    glu_col0, gate_col0 = qkv_width, qkv_width + 2 * D
    shard_major = lambda g: g.reshape(N_CHIPS, g.shape[0] // N_CHIPS, g.shape[1])

    h1 = _rms_fwd_call(xs, P["norm_mix_pre"])
    buckets = _bucket_tables()
    bias = _bias_table_call(P["rel_bias"] + 0.0 * h1[0, 0].astype(F32), buckets)
    P = dict(P, **late_weights("in", bias))
    proj = _matmul(h1, P["w_in"], "nn", "proj_in")
    parts = []
    for g in range(N_GROUPS):
        parts += _attn_fwd_call(proj, bias, g)
    a, a_bf, lse = _attn_merge_call(parts)
    P = dict(P, **late_weights("mix", a_bf))
    y_a = _matmul(a_bf, P["w_attn_out"], "nn", "attn_out")
    c1 = _conv_fwd_call(proj, glu_col0, P["conv_dw_w"], P["conv_dw_b"])
    cact = _ln_silu_call(c1, P["conv_ln_g"], P["conv_ln_b"])
    y_c = _matmul(cact, P["conv_pw_w"], "nn", "conv_pw")
    mixed = _mix_call(proj, gate_col0, P["b_gate"], y_a, y_c)
    out = _matmul(mixed, P["w_out"], "nn", "mix_out")
    x1, h2 = _res1_call(xs, out, P["norm_mix_post"], P["norm_ffn_pre"])
    P = dict(P, **late_weights("ffn", h2))
    u = _matmul(h2, P["w_up"], "nn", "ffn_up")
    f = _ffn_fwd_call(u, P["ffn_conv_w"], P["ffn_conv_b"])
    yff = _matmul(f, P["w_down"], "nn", "ffn_down")
    loss_tile, dx2, dyff, dg_ffn_post = _loss_call(yff, x1, P["norm_ffn_post"], target)

    G = {}
    G["norm_ffn_post"] = dg_ffn_post
    zero = on_grad("w_down", shard_major(_matmul(f, dyff, "tn", "ffn_down_dw")))
    df = _matmul(dyff, P["w_down"], "nt", "ffn_down_dx")
    du, dwg, dwv, dbg, dbv = _ffn_bwd_call(u, P["ffn_conv_w"], P["ffn_conv_b"] + zero, df)
    G["ffn_conv_w"] = jnp.concatenate([dwg[:FFN_CONV_WIDTH], dwv[:FFN_CONV_WIDTH]], axis=1)
    G["ffn_conv_b"] = jnp.concatenate([dbg, dbv], axis=1)
    zero = on_grad("w_up", _matmul(h2, du, "tn", "ffn_up_dw", out_shards=True))
    dh2 = _matmul(du, P["w_up"], "nt", "ffn_up_dx")
    dx1, dout, G["norm_ffn_pre"], G["norm_mix_post"] = _mid_bwd_call(x1, P["norm_ffn_pre"] + zero, dh2, dx2, out, P["norm_mix_post"])
    zero = on_grad("w_out", shard_major(_matmul(mixed, dout, "tn", "mix_out_dw")))
    dmixed = _matmul(dout, P["w_out"], "nt", "mix_out_dx")
    dya, dyc, dga, dgc, dba, dbc = _mix_bwd_call(dmixed, proj, gate_col0, P["b_gate"] + zero, y_a, y_c)
    G["b_gate"] = jnp.concatenate([dba, dbc], axis=1)
    zero = on_grad("w_attn_out", _matmul(a_bf, dya, "tn", "attn_out_dw", out_shards=True))
    zero = zero + on_grad("conv_pw_w", shard_major(_matmul(cact, dyc, "tn", "conv_pw_dw")))
    da = _matmul(dya, P["w_attn_out"], "nt", "attn_out_dx")
    dcact = _matmul(dyc, P["conv_pw_w"], "nt", "conv_pw_dx")
    dc1, G["conv_ln_g"], G["conv_ln_b"] = _ln_silu_bwd_call(c1, P["conv_ln_g"] + zero, P["conv_ln_b"], dcact)
    dval, dgate, dw_dw, G["conv_dw_b"] = _conv_bwd_call(proj, glu_col0, P["conv_dw_w"], dc1)
    G["conv_dw_w"] = dw_dw[:CONV_WIDTH]
    delta = _attn_delta_call(a, da)
    dqs, dks, dvs, dbs = [], [], [], []
    for g in range(N_GROUPS):
        dq, dk, dv, db = _attn_bwd_call(proj, bias, da, lse, delta, g)
        dqs.append(dq)
        dks.append(dk)
        dvs.append(dv)
        dbs.append(db)
    G["rel_bias"] = _bias_grad_call(jnp.concatenate(dbs, axis=0), buckets)
    dproj = _dproj_call(dqs + dks + dvs, [dval, dgate, dga, dgc])
    zero = on_grad("w_in", _matmul(h1, dproj, "tn", "proj_in_dw", out_shards=True))
    dh1 = _matmul(dproj, P["w_in"], "nt", "proj_in_dx")
    zero = zero + on_grad(None, dh1)
    grad_x, G["norm_mix_pre"] = _in_bwd_call(xs, P["norm_mix_pre"] + zero, dh1, dx1)
    return loss_tile, grad_x, G


def kernel(x, w_in, b_gate, rel_bias, w_attn_out, conv_dw_w, conv_dw_b, conv_ln_g, conv_ln_b, conv_pw_w, w_out, norm_mix_pre, norm_mix_post, norm_ffn_pre, norm_ffn_post, w_up, ffn_conv_w, ffn_conv_b, w_down, loss_target, m_w_in, m_b_gate, m_rel_bias, m_w_attn_out, m_conv_dw_w, m_conv_dw_b, m_conv_ln_g, m_conv_ln_b, m_conv_pw_w, m_w_out, m_norm_mix_pre, m_norm_mix_post, m_norm_ffn_pre, m_norm_ffn_post, m_w_up, m_ffn_conv_w, m_ffn_conv_b, m_w_down, v_w_in, v_b_gate, v_rel_bias, v_w_attn_out, v_conv_dw_w, v_conv_dw_b, v_conv_ln_g, v_conv_ln_b, v_conv_pw_w, v_w_out, v_norm_mix_pre, v_norm_mix_post, v_norm_ffn_pre, v_norm_ffn_post, v_w_up, v_ffn_conv_w, v_ffn_conv_b, v_w_down):
    weights = dict(w_in=w_in, b_gate=b_gate, rel_bias=rel_bias, w_attn_out=w_attn_out, conv_dw_w=conv_dw_w, conv_dw_b=conv_dw_b,
                   conv_ln_g=conv_ln_g, conv_ln_b=conv_ln_b, conv_pw_w=conv_pw_w, w_out=w_out, norm_mix_pre=norm_mix_pre,
                   norm_mix_post=norm_mix_post, norm_ffn_pre=norm_ffn_pre, norm_ffn_post=norm_ffn_post, w_up=w_up,
                   ffn_conv_w=ffn_conv_w, ffn_conv_b=ffn_conv_b, w_down=w_down)
    m_in = dict(w_in=m_w_in, b_gate=m_b_gate, rel_bias=m_rel_bias, w_attn_out=m_w_attn_out, conv_dw_w=m_conv_dw_w,
                conv_dw_b=m_conv_dw_b, conv_ln_g=m_conv_ln_g, conv_ln_b=m_conv_ln_b, conv_pw_w=m_conv_pw_w, w_out=m_w_out,
                norm_mix_pre=m_norm_mix_pre, norm_mix_post=m_norm_mix_post, norm_ffn_pre=m_norm_ffn_pre,
                norm_ffn_post=m_norm_ffn_post, w_up=m_w_up, ffn_conv_w=m_ffn_conv_w, ffn_conv_b=m_ffn_conv_b, w_down=m_w_down)
    v_in = dict(w_in=v_w_in, b_gate=v_b_gate, rel_bias=v_rel_bias, w_attn_out=v_w_attn_out, conv_dw_w=v_conv_dw_w,
                conv_dw_b=v_conv_dw_b, conv_ln_g=v_conv_ln_g, conv_ln_b=v_conv_ln_b, conv_pw_w=v_conv_pw_w, w_out=v_w_out,
                norm_mix_pre=v_norm_mix_pre, norm_mix_post=v_norm_mix_post, norm_ffn_pre=v_norm_ffn_pre,
                norm_ffn_post=v_norm_ffn_post, w_up=v_w_up, ffn_conv_w=v_ffn_conv_w, ffn_conv_b=v_ffn_conv_b, w_down=v_w_down)
    names = list(weights)
    xi, yi, ci = _position()
    chip = 2 * xi + yi
    core_arr = jnp.reshape(ci, (1,)).astype(jnp.int32)

    xs = x[0]
    target = loss_target[0]
    S, D = xs.shape

    big = ["w_in", "w_attn_out", "conv_pw_w", "w_out", "w_up", "w_down"]
    row_sharded = ("conv_pw_w", "w_out", "w_down")
    bf16_shard = {k: weights[k][0].astype(BF16) for k in big}
    natural = lambda k, g: g.reshape(-1, g.shape[2]) if k in row_sharded else g
    first_srcs = [bf16_shard["w_in"], conv_dw_w[0], ffn_conv_w[0]]
    first_lands = [lax.empty((N_CHIPS,) + s.shape, s.dtype) for s in first_srcs]
    (first_hop,) = _split_start("gather_in_start", [(first_srcs, first_lands, 4 * len(first_srcs), _first_hop_copies)], core_arr)
    launched = first_hop["token"]
    late_sets = dict(mix=["w_attn_out", "conv_pw_w", "w_out"], ffn=["w_up", "w_down"])
    late_groups = []
    for keys in late_sets.values():
        srcs = [bf16_shard[k] for k in keys]
        late_groups.append((srcs, [lax.empty((N_CHIPS,) + s.shape, BF16) for s in srcs], 4 * len(keys), _gather_copies))
    started = {}

    def late_weights(tag, after):
        if tag == "in":
            w_in_halves, dw4, fc4 = _split_wait("gather_in_wait", first_hop, _first_hop_copies, after)[len(first_srcs):]
            (second_hop,) = _split_start("gather_in_pass_start", [([], [w_in_halves], 3, _second_hop_copies)], dw4)
            (w_in_full,) = _split_wait("gather_in_pass_wait", second_hop, _second_hop_copies, second_hop["tile"])
            started.update(zip(late_sets, _split_start("gather_late_start", late_groups, w_in_full)))
            return dict(w_in=w_in_full, conv_dw_w=jnp.concatenate(list(dw4), axis=1), ffn_conv_w=jnp.concatenate(list(fc4), axis=1))
        landed = _split_wait(f"gather_{tag}_wait", started[tag], _gather_copies, after)[len(late_sets[tag]):]
        return {k: natural(k, g) for k, g in zip(late_sets[tag], landed)}

    chip_core = jnp.stack([chip, ci]).astype(jnp.int32)
    exchanging, pending = {}, {}

    def launch(tag, g3, after):
        keys, groups, partial = [], [], {}
        for k in list(exchanging):
            gk, r1 = _split_wait(f"sibling_exchange_wait_{k}", exchanging.pop(k), _sibling_copies, after)
            partial[k], s16 = _pair_sum_call(gk, r1, core_arr, f"pair_sum_{k}")
            keys.append(k)
            groups.append(([s16], [lax.empty((3,) + s16.shape[1:], BF16)], 3, _exchange_copies))
        if g3 is not None:
            groups.append(([g3], [lax.empty((N_CHIPS, g3.shape[1] // 2, g3.shape[2]), F32)], 1, _sibling_copies))
        begun = _split_start(f"grad_exchange_start_{tag}", groups, core_arr)
        for k, st in zip(keys, begun):
            pending[k] = (partial[k], st)
        if g3 is not None:
            exchanging[tag] = begun[-1]
        return begun[0]["token"]

    def on_grad(k, g3):
        if k is None:
            return launch("last", None, g3[:SUBLANES, :LANES])
        return launch(k, g3, g3[0, :SUBLANES, :LANES])

    def finish(keys, after, tag):
        halves = []
        for k in keys:
            s32, st = pending[k]
            recv2 = _split_wait(f"chip_exchange_wait_{k}", st, _exchange_copies, after)[1]
            halves.append(_chip_sum_call(s32, recv2, chip_core, f"chip_sum_{k}"))
        return dict(zip(keys, _sibling_assemble_call(halves, f"grad_sibling_assemble_{tag}")))

    P = dict(b_gate=b_gate, rel_bias=rel_bias, conv_dw_b=conv_dw_b, conv_ln_g=conv_ln_g, conv_ln_b=conv_ln_b,
             norm_mix_pre=norm_mix_pre + launched, norm_mix_post=norm_mix_post, norm_ffn_pre=norm_ffn_pre,
             norm_ffn_post=norm_ffn_post, ffn_conv_b=ffn_conv_b)
    loss_tile, grad_x, G = _local_step(xs, target, P, late_weights, on_grad)

    small = [k for k in names if k not in big]
    packed = _pack([loss_tile[:1]] + [G[k] for k in small])
    (allsum,) = _split_start("small_grad_allsum_start",
                             [([packed], [jnp.zeros((N_DEVICES,) + packed.shape, F32)], N_DEVICES - 1, _allsum_copies)], core_arr)

    reduced, grads, deltas, new_m, new_v = {}, {}, {}, {}, {}

    def update(keys):
        for k in keys:
            gk, d, mn, vn = _adamw_call(weights[k][0], reduced[k], m_in[k][0], v_in[k][0], f"adamw_{k}")
            grads[k], deltas[k], new_m[k], new_v[k] = gk[None], d[None], mn[None], vn[None]

    others = [k for k in big if k != "w_in"]
    reduced.update(finish(others, allsum["tile"], "others"))
    update(others)
    reduced.update(finish(["w_in"], deltas["w_up"], "w_in"))
    update(["w_in"])

    me = jnp.reshape(4 * xi + 2 * yi + ci, (1,)).astype(jnp.int32)
    mine, landed = _split_wait("small_grad_allsum_wait", allsum, _allsum_copies, deltas["w_in"])
    summed_block = _ordered_sum_call(mine, landed, me)
    loss_row, *summed = _unpack(summed_block, [(1, LANES)] + [G[k].shape for k in small])
    loss = loss_row[0, 0]
    for k, gsum in zip(small, summed):
        if k in ("conv_dw_w", "ffn_conv_w"):
            cols = weights[k].shape[2]
            reduced[k] = lax.dynamic_slice_in_dim(gsum, chip * cols, cols, axis=1)
        else:
            reduced[k] = gsum
    for k in small:
        grads[k] = reduced[k].reshape(weights[k].shape)
    ds, mns, vns = _adamw_small_call([weights[k] for k in small], [grads[k] for k in small],
                                     [m_in[k] for k in small], [v_in[k] for k in small])
    deltas.update(zip(small, ds))
    new_m.update(zip(small, mns))
    new_v.update(zip(small, vns))

    return (loss, grad_x[None], *[grads[k] for k in names], *[deltas[k] for k in names],
            *[new_m[k] for k in names], *[new_v[k] for k in names])
```

```python
import functools
import math

import jax
import jax.numpy as jnp
import numpy as np
from jax import lax
from jax.experimental import pallas as pl
from jax.experimental.pallas import tpu as pltpu

F32 = jnp.float32
BF16 = jnp.bfloat16
MESH = pl.DeviceIdType.MESH

HEAD_DIM = 128
HEADS_PER_GROUP = 4
DILATED_PATTERNS = ((128, 1), (512, 4), (2048, 16))
N_GROUPS = 3
N_HEADS = N_GROUPS * HEADS_PER_GROUP
SPAN = 128
GROUP_WIDTH = HEADS_PER_GROUP * HEAD_DIM
CONV_WIDTH = 31
FFN_CONV_WIDTH = 3
N_BUCKETS = 32
MAX_DISTANCE = 2048
RMS_EPS = 1e-6
LN_EPS = 1e-5
NEG_INF = -1e30
ADAM_LR = 0.001
ADAM_B1 = 0.9
ADAM_B2 = 0.999
ADAM_EPS = 1e-08
ADAM_WD = 0.01
ADAM_STEP = 10

LANES = 128
SUBLANES = 8
ROW_TILE = 512
GATE_ROWS, GATE_COLS = 512, 512
TIME_BLOCK = 128
CONV_PAD = 32
FFN_PAD = 8
VMEM_LIMIT = 56 << 20


def _params(sem=None, vmem=None):
    kw = {}
    if sem is not None:
        kw["dimension_semantics"] = sem
    if vmem is not None:
        kw["vmem_limit_bytes"] = vmem
    return pltpu.CompilerParams(**kw)


def _pick(n, cands):
    for c in cands:
        if n % c == 0:
            return c
    return n


ELEMENTWISE_TILE_BYTES = 3 << 19


def _row_tile(rows, cols):
    for align in (16, SUBLANES):
        fits = [t for t in range(align, rows + 1, align) if rows % t == 0 and t * cols * 4 <= ELEMENTWISE_TILE_BYTES]
        if fits:
            return max(fits)
    return SUBLANES


N_CHIPS = 4
M_TILES = (1024, 1408, 512, 256, 128)
N_TILES = (1024, 512, 1408, 256, 128)
K_TILES = (2176, 2048, 1408, 1024, 512, 256, 128)


def _matmul(a, b, mode, name, out_shards=False, tm=None):
    assert a.dtype == BF16 and b.dtype == BF16, (name, a.dtype, b.dtype)
    b3 = b.ndim == 3
    tn = tk = None
    halves = None
    if mode == "nn":
        M, K = a.shape
        N = b.shape[-1] * (N_CHIPS if b3 else 1)
        tn = b.shape[-1] if b3 else None
    elif mode == "nt":
        if a.ndim == 3:
            halves = a.shape[2]
        M, K = a.shape[-2], a.shape[-1] * (a.shape[0] if a.ndim == 3 else 1)
        N = b.shape[-2]
        tk = b.shape[-1] if b3 else None
    else:
        if b3:
            halves = b.shape[2]
        K, M = a.shape
        N = b.shape[-1] * (b.shape[0] if b3 else 1)
        tn = N // N_CHIPS if out_shards else None
    tm = tm or _pick(M, M_TILES)
    tn = tn or _pick(N, N_TILES)
    tk = tk or _pick(K, K_TILES)
    nk = K // tk
    dn = {"nn": (((1,), (0,)), ((), ())), "nt": (((1,), (1,)), ((), ())), "tn": (((0,), (0,)), ((), ()))}[mode]

    def body(a_ref, b_ref, o_ref):
        if nk == 1:
            o_ref[...] = lax.dot_general(a_ref[...], b_ref[...], dn, preferred_element_type=F32)
        else:
            @pl.when(pl.program_id(2) == 0)
            def _():
                o_ref[...] = jnp.zeros_like(o_ref)

            o_ref[...] += lax.dot_general(a_ref[...], b_ref[...], dn, preferred_element_type=F32)

    if mode == "tn":
        a_spec = pl.BlockSpec((tk, tm), lambda i, j, k: (k, i))
    elif halves:
        per = halves // tk
        a_spec = pl.BlockSpec((None, tm, tk), lambda i, j, k: (k // per, i, k % per))
    else:
        a_spec = pl.BlockSpec((tm, tk), lambda i, j, k: (i, k))
    if mode == "nn":
        b_spec = pl.BlockSpec((None, tk, tn), lambda i, j, k: (j, k, 0)) if b3 else pl.BlockSpec((tk, tn), lambda i, j, k: (k, j))
    elif mode == "nt":
        b_spec = pl.BlockSpec((None, tn, tk), lambda i, j, k: (k, j, 0)) if b3 else pl.BlockSpec((tn, tk), lambda i, j, k: (j, k))
    elif halves:
        per = halves // tn
        b_spec = pl.BlockSpec((None, tk, tn), lambda i, j, k: (j // per, k, j % per))
    else:
        b_spec = pl.BlockSpec((tk, tn), lambda i, j, k: (k, j))
    if out_shards:
        out_spec = pl.BlockSpec((None, tm, tn), lambda i, j, k: (j, i, 0))
        out_shape = jax.ShapeDtypeStruct((N_CHIPS, M, tn), F32)
    else:
        out_spec = pl.BlockSpec((tm, tn), lambda i, j, k: (i, j))
        out_shape = jax.ShapeDtypeStruct((M, N), F32)
    return pl.pallas_call(
        body, name=name, grid=(M // tm, N // tn, nk),
        in_specs=[a_spec, b_spec], out_specs=out_spec, out_shape=out_shape,
        compiler_params=_params(("parallel", "parallel", "arbitrary"), VMEM_LIMIT),
    )(a, b)


def _rms(x, g):
    r = lax.rsqrt(jnp.mean(x * x, axis=-1, keepdims=True) + RMS_EPS)
    return x * r * g


def _rms_bwd(x, g, dy):
    r = lax.rsqrt(jnp.mean(x * x, axis=-1, keepdims=True) + RMS_EPS)
    n = x * r
    dn = dy * g
    dx = r * (dn - n * jnp.mean(dn * n, axis=-1, keepdims=True))
    return dx, jnp.sum(dy * n, axis=0, keepdims=True)


def _sigmoid(x):
    return 1.0 / (1.0 + jnp.exp(-x))


_GELU_C = math.sqrt(2.0 / math.pi)


def _gelu(x):
    return 0.5 * x * (1.0 + jnp.tanh(_GELU_C * (x + 0.044715 * x * x * x)))


def _gelu_and_grad(x):
    x2 = x * x
    t = jnp.tanh(_GELU_C * x * (1.0 + 0.044715 * x2))
    half = 0.5 * (1.0 + t)
    return x * half, half + (0.5 * _GELU_C) * x * (1.0 - t * t) * (1.0 + (3.0 * 0.044715) * x2)


def _row_spec(width, col_block=0):
    return pl.BlockSpec((ROW_TILE, width), lambda i: (i, col_block))


def _vec_spec(width, col_block=0):
    return pl.BlockSpec((1, width), lambda i: (0, col_block))


def _accumulate(ref, part):
    @pl.when(pl.program_id(0) == 0)
    def _():
        ref[...] = part

    @pl.when(pl.program_id(0) > 0)
    def _():
        ref[...] += part


def _rms_fwd_call(x, g):
    S, D = x.shape

    def body(x_ref, g_ref, h_ref):
        h_ref[...] = _rms(x_ref[...], g_ref[...]).astype(BF16)

    return pl.pallas_call(
        body, name="rms_mix_pre", grid=(S // ROW_TILE,),
        in_specs=[_row_spec(D), _vec_spec(D)], out_specs=_row_spec(D),
        out_shape=jax.ShapeDtypeStruct((S, D), BF16),
        compiler_params=_params(("parallel",)),
    )(x, g)


def _ln_silu_call(c1, g, b):
    S, C = c1.shape

    def body(c_ref, g_ref, b_ref, o_ref):
        xv = c_ref[...]
        mu = jnp.mean(xv, axis=-1, keepdims=True)
        xc = xv - mu
        var = jnp.mean(xc * xc, axis=-1, keepdims=True)
        z = xc * lax.rsqrt(var + LN_EPS) * g_ref[...] + b_ref[...]
        o_ref[...] = (z * _sigmoid(z)).astype(BF16)

    return pl.pallas_call(
        body, name="conv_ln_silu", grid=(S // ROW_TILE,),
        in_specs=[_row_spec(C), _vec_spec(C), _vec_spec(C)], out_specs=_row_spec(C),
        out_shape=jax.ShapeDtypeStruct((S, C), BF16),
        compiler_params=_params(("parallel",)),
    )(c1, g, b)


def _ln_silu_bwd_call(c1, g, b, dc):
    S, C = c1.shape

    def body(c_ref, g_ref, b_ref, dc_ref, dx_ref, dg_ref, db_ref):
        xv = c_ref[...]
        mu = jnp.mean(xv, axis=-1, keepdims=True)
        xc = xv - mu
        rs = lax.rsqrt(jnp.mean(xc * xc, axis=-1, keepdims=True) + LN_EPS)
        xh = xc * rs
        z = xh * g_ref[...] + b_ref[...]
        sg = _sigmoid(z)
        dz = dc_ref[...] * (sg * (1.0 + z * (1.0 - sg)))
        dxh = dz * g_ref[...]
        dx_ref[...] = rs * (dxh - jnp.mean(dxh, axis=-1, keepdims=True) - xh * jnp.mean(dxh * xh, axis=-1, keepdims=True))
        _accumulate(dg_ref, jnp.sum(dz * xh, axis=0, keepdims=True))
        _accumulate(db_ref, jnp.sum(dz, axis=0, keepdims=True))

    return pl.pallas_call(
        body, name="conv_ln_silu_bwd", grid=(S // ROW_TILE,),
        in_specs=[_row_spec(C), _vec_spec(C), _vec_spec(C), _row_spec(C)],
        out_specs=[_row_spec(C), _vec_spec(C), _vec_spec(C)],
        out_shape=[jax.ShapeDtypeStruct((S, C), F32), jax.ShapeDtypeStruct((1, C), F32), jax.ShapeDtypeStruct((1, C), F32)],
        compiler_params=_params(("arbitrary",)),
    )(c1, g, b, dc)


def _mix_call(proj, gate_col0, b_gate, y_a, y_c):
    S, D = y_a.shape
    w = GATE_COLS
    nc = D // w
    ga0, gc0 = gate_col0 // w, (gate_col0 + D) // w

    def body(ga_ref, gc_ref, ba_ref, bc_ref, ya_ref, yc_ref, o_ref):
        o_ref[...] = (_sigmoid(ga_ref[...] + ba_ref[...]) * ya_ref[...]
                      + _sigmoid(gc_ref[...] + bc_ref[...]) * yc_ref[...]).astype(BF16)

    tile = lambda off: pl.BlockSpec((GATE_ROWS, w), lambda i, j: (i, off + j))
    vec = lambda off: pl.BlockSpec((1, w), lambda i, j: (0, off + j))
    return pl.pallas_call(
        body, name="gate_mix", grid=(S // GATE_ROWS, nc),
        in_specs=[tile(ga0), tile(gc0), vec(0), vec(nc), tile(0), tile(0)],
        out_specs=tile(0), out_shape=jax.ShapeDtypeStruct((S, D), BF16),
        compiler_params=_params(("parallel", "parallel")),
    )(proj, proj, b_gate, b_gate, y_a, y_c)


def _mix_bwd_call(dmixed, proj, gate_col0, b_gate, y_a, y_c):
    S, D = y_a.shape
    w = GATE_COLS
    nc = D // w
    ga0, gc0 = gate_col0 // w, (gate_col0 + D) // w

    def body(dm_ref, ga_ref, gc_ref, ba_ref, bc_ref, ya_ref, yc_ref, dya_ref, dyc_ref, dga_ref, dgc_ref, dba_ref, dbc_ref):
        dm = dm_ref[...]
        sa = _sigmoid(ga_ref[...] + ba_ref[...])
        sc = _sigmoid(gc_ref[...] + bc_ref[...])
        dya_ref[...] = (dm * sa).astype(BF16)
        dyc_ref[...] = (dm * sc).astype(BF16)
        dga = dm * ya_ref[...] * sa * (1.0 - sa)
        dgc = dm * yc_ref[...] * sc * (1.0 - sc)
        dga_ref[...] = dga.astype(BF16)
        dgc_ref[...] = dgc.astype(BF16)
        pa = jnp.sum(dga, axis=0, keepdims=True)
        pc = jnp.sum(dgc, axis=0, keepdims=True)

        @pl.when(pl.program_id(1) == 0)
        def _():
            dba_ref[...] = pa
            dbc_ref[...] = pc

        @pl.when(pl.program_id(1) > 0)
        def _():
            dba_ref[...] += pa
            dbc_ref[...] += pc

    tile = lambda off: pl.BlockSpec((GATE_ROWS, w), lambda j, i: (i, off + j))
    vec = lambda off: pl.BlockSpec((1, w), lambda j, i: (0, off + j))
    return pl.pallas_call(
        body, name="gate_mix_bwd", grid=(nc, S // GATE_ROWS),
        in_specs=[tile(0), tile(ga0), tile(gc0), vec(0), vec(nc), tile(0), tile(0)],
        out_specs=[tile(0), tile(0), tile(0), tile(0), vec(0), vec(0)],
        out_shape=[jax.ShapeDtypeStruct((S, D), BF16)] * 4 + [
                   jax.ShapeDtypeStruct((1, D), F32), jax.ShapeDtypeStruct((1, D), F32)],
        compiler_params=_params(("parallel", "arbitrary")),
    )(dmixed, proj, proj, b_gate, b_gate, y_a, y_c)


def _res1_call(x, out, g_post, g_pre):
    S, D = x.shape

    def body(x_ref, o_ref, gp_ref, gq_ref, x1_ref, h2_ref):
        x1 = x_ref[...] + _rms(o_ref[...], gp_ref[...])
        x1_ref[...] = x1
        h2_ref[...] = _rms(x1, gq_ref[...]).astype(BF16)

    return pl.pallas_call(
        body, name="residual_mix", grid=(S // ROW_TILE,),
        in_specs=[_row_spec(D), _row_spec(D), _vec_spec(D), _vec_spec(D)],
        out_specs=[_row_spec(D), _row_spec(D)],
        out_shape=[jax.ShapeDtypeStruct((S, D), F32), jax.ShapeDtypeStruct((S, D), BF16)],
        compiler_params=_params(("parallel",)),
    )(x, out, g_post, g_pre)


def _loss_call(y, x1, g_post, target):
    S, D = y.shape

    def body(y_ref, x1_ref, g_ref, t_ref, loss_ref, dx_ref, dy_ref, dg_ref):
        yv, gv = y_ref[...], g_ref[...]
        err = x1_ref[...] + _rms(yv, gv) - t_ref[...]
        dx2 = err * (1.0 / D)
        dx_ref[...] = dx2
        dy, dg = _rms_bwd(yv, gv, dx2)
        dy_ref[...] = dy.astype(BF16)
        _accumulate(dg_ref, dg)
        part = 0.5 * jnp.sum(jnp.mean(err * err, axis=-1, keepdims=True), axis=0, keepdims=True)
        _accumulate(loss_ref, jnp.broadcast_to(part, (SUBLANES, LANES)))

    return pl.pallas_call(
        body, name="residual_ffn_loss", grid=(S // ROW_TILE,),
        in_specs=[_row_spec(D), _row_spec(D), _vec_spec(D), _row_spec(D)],
        out_specs=[pl.BlockSpec((SUBLANES, LANES), lambda i: (0, 0)), _row_spec(D), _row_spec(D), _vec_spec(D)],
        out_shape=[jax.ShapeDtypeStruct((SUBLANES, LANES), F32), jax.ShapeDtypeStruct((S, D), F32),
                   jax.ShapeDtypeStruct((S, D), BF16), jax.ShapeDtypeStruct((1, D), F32)],
        compiler_params=_params(("arbitrary",)),
    )(y, x1, g_post, target)


def _mid_bwd_call(x1, g_pre, dh2, dx2, out, g_post):
    S, D = x1.shape

    def body(x1_ref, gq_ref, dh_ref, dx2_ref, o_ref, gp_ref, dx1_ref, do_ref, dgq_ref, dgp_ref):
        d, dgq = _rms_bwd(x1_ref[...], gq_ref[...], dh_ref[...])
        dx1 = dx2_ref[...] + d
        dx1_ref[...] = dx1
        do, dgp = _rms_bwd(o_ref[...], gp_ref[...], dx1)
        do_ref[...] = do.astype(BF16)
        _accumulate(dgq_ref, dgq)
        _accumulate(dgp_ref, dgp)

    return pl.pallas_call(
        body, name="residual_mix_bwd", grid=(S // ROW_TILE,),
        in_specs=[_row_spec(D), _vec_spec(D), _row_spec(D), _row_spec(D), _row_spec(D), _vec_spec(D)],
        out_specs=[_row_spec(D), _row_spec(D), _vec_spec(D), _vec_spec(D)],
        out_shape=[jax.ShapeDtypeStruct((S, D), F32), jax.ShapeDtypeStruct((S, D), BF16)] + [jax.ShapeDtypeStruct((1, D), F32)] * 2,
        compiler_params=_params(("arbitrary",)),
    )(x1, g_pre, dh2, dx2, out, g_post)


def _in_bwd_call(x, g, dh1, dx1):
    S, D = x.shape

    def body(x_ref, g_ref, dh_ref, dx1_ref, gx_ref, dg_ref):
        d, dg = _rms_bwd(x_ref[...], g_ref[...], dh_ref[...])
        gx_ref[...] = dx1_ref[...] + d
        _accumulate(dg_ref, dg)

    return pl.pallas_call(
        body, name="rms_mix_pre_bwd", grid=(S // ROW_TILE,),
        in_specs=[_row_spec(D), _vec_spec(D), _row_spec(D), _row_spec(D)],
        out_specs=[_row_spec(D), _vec_spec(D)],
        out_shape=[jax.ShapeDtypeStruct((S, D), F32), jax.ShapeDtypeStruct((1, D), F32)],
        compiler_params=_params(("arbitrary",)),
    )(x, g, dh1, dx1)


def _bucket_table(dilation):
    qi = np.arange(SPAN)[:, None]
    ki = np.arange(2 * SPAN)[None, :]
    dist = np.maximum(qi + SPAN - ki, 0) * dilation
    max_exact = N_BUCKETS // 2
    d = np.maximum(dist, 1).astype(np.float64)
    large = max_exact + (np.log(d / max_exact) / math.log(MAX_DISTANCE / max_exact) * (N_BUCKETS - max_exact)).astype(np.int32)
    large = np.minimum(large, N_BUCKETS - 1)
    return np.where(dist < max_exact, dist, large).astype(np.int32)


def _bucket_tables():
    return jnp.asarray(np.stack([_bucket_table(r) for _, r in DILATED_PATTERNS]))


def _bias_table_call(rel_bias, buckets):
    def body(rb_ref, bk_ref, o_ref):
        for h in range(N_HEADS):
            bk = bk_ref[h // HEADS_PER_GROUP]

            def step(b, acc):
                return jnp.where(bk == b, rb_ref[b, h], acc)

            o_ref[h] = lax.fori_loop(0, N_BUCKETS, step, jnp.zeros((SPAN, 2 * SPAN), F32))

    return pl.pallas_call(
        body, name="rel_bias_table",
        in_specs=[pl.BlockSpec(memory_space=pltpu.SMEM), pl.BlockSpec(memory_space=pltpu.VMEM)],
        out_specs=pl.BlockSpec(memory_space=pltpu.VMEM),
        out_shape=jax.ShapeDtypeStruct((N_HEADS, SPAN, 2 * SPAN), F32),
    )(rel_bias, buckets)


def _bias_grad_call(dbias, buckets):
    def body(db_ref, bk_ref, o_ref, rows_ref):
        for h in range(N_HEADS):
            bk = bk_ref[h // HEADS_PER_GROUP]
            dv = db_ref[h]

            def step(b, carry):
                rows_ref[h, b] = jnp.sum(jnp.where(bk == b, dv, 0.0), axis=0, keepdims=True)
                return carry

            lax.fori_loop(0, N_BUCKETS, step, 0)
        o_ref[...] = jnp.sum(rows_ref[...], axis=-1, keepdims=True)

    out = pl.pallas_call(
        body, name="rel_bias_grad",
        in_specs=[pl.BlockSpec(memory_space=pltpu.VMEM), pl.BlockSpec(memory_space=pltpu.VMEM)],
        out_specs=pl.BlockSpec(memory_space=pltpu.VMEM),
        out_shape=jax.ShapeDtypeStruct((N_HEADS, N_BUCKETS, 1, 1), F32),
        scratch_shapes=[pltpu.VMEM((N_HEADS, N_BUCKETS, 1, 2 * SPAN), F32)],
    )(dbias, buckets)
    return out.reshape(N_HEADS, N_BUCKETS).T


def _dot_nt(a, b):
    return lax.dot_general(a, b, (((1,), (1,)), ((), ())), preferred_element_type=F32)


def _dot_nn(a, b):
    return lax.dot_general(a, b, (((1,), (0,)), ((), ())), preferred_element_type=F32)


def _dot_tn(a, b):
    return lax.dot_general(a, b, (((0,), (0,)), ((), ())), preferred_element_type=F32)


def _band_masks(n, nb):
    qi = lax.broadcasted_iota(jnp.int32, (SPAN, SPAN), 0)
    ki = lax.broadcasted_iota(jnp.int32, (SPAN, SPAN), 1)
    prev_ok = jnp.logical_and(ki >= qi, n > 0)
    cur_ok = ki <= qi
    next_ok = jnp.logical_and(ki >= qi, n < nb - 1)
    return prev_ok, cur_ok, next_ok


def _wide_band_mask(n):
    qi = lax.broadcasted_iota(jnp.int32, (SPAN, 2 * SPAN), 0)
    ki = lax.broadcasted_iota(jnp.int32, (SPAN, 2 * SPAN), 1)
    prev_ok = jnp.logical_and(jnp.logical_and(ki < SPAN, ki >= qi), n > 0)
    cur_ok = jnp.logical_and(ki >= SPAN, ki - SPAN <= qi)
    return jnp.logical_or(prev_ok, cur_ok)


def _attn_plan(S, group):
    r = DILATED_PATTERNS[group][1]
    hp, per = (HEADS_PER_GROUP, 1) if r == 1 else (2, 4)
    return r, S // (r * SPAN), hp, per


def _residue_rows(rho, r):
    return slice(None) if r == 1 else pl.ds(rho, SPAN, stride=r)


def _for_residues(r, per, fn):
    if r == per:
        for u in range(per):
            fn(u)
        return

    def step(i, carry):
        for u in range(per):
            fn(i * per + u)
        return carry

    lax.fori_loop(0, r // per, step, 0)


def _attn_fwd_call(proj, bias, group):
    S = proj.shape[0]
    r, nb, hp, per = _attn_plan(S, group)
    scale = HEAD_DIM ** -0.5
    kinds = ("q", "kp", "kc", "vp", "vc") if nb > 1 else ("q", "kc", "vc")

    def body(*refs):
        ins = {kind: refs[i * hp:(i + 1) * hp] for i, kind in enumerate(kinds)}
        b_ref, o_ref, lse_ref = refs[len(kinds) * hp:]
        n = pl.program_id(1)
        prev_ok, cur_ok, _ = _band_masks(n, nb)

        band_ok = _wide_band_mask(n) if nb > 1 else cur_ok

        def residue(rho):
            rows = _residue_rows(rho, r)
            for j in range(hp):
                get = lambda kind: ins[kind][j][rows, :].astype(BF16)
                q = get("q")
                if nb > 1:
                    keys, vals, bias_j = jnp.concatenate([get("kp"), get("kc")], axis=0), jnp.concatenate([get("vp"), get("vc")], axis=0), b_ref[j]
                else:
                    keys, vals, bias_j = get("kc"), get("vc"), b_ref[j, :, SPAN:]
                s = jnp.where(band_ok, _dot_nt(q, keys) * scale + bias_j, NEG_INF)
                m = jnp.max(s, axis=-1, keepdims=True)
                p = jnp.exp(s - m)
                den = jnp.sum(p, axis=-1, keepdims=True)
                o_ref[j, rows, :] = _dot_nn(p.astype(BF16), vals) / den
                lse_ref[j, rows, :] = jnp.broadcast_to(m + jnp.log(den), (SPAN, HEAD_DIM))

        _for_residues(r, per, residue)

    in_specs = [_head_spec(r, nb, hp, kind, group, jj) for kind in kinds for jj in range(hp)]
    in_specs.append(pl.BlockSpec((hp, SPAN, 2 * SPAN), lambda j, n: (group * (HEADS_PER_GROUP // hp) + j, 0, 0)))
    out = pl.BlockSpec((hp, r * SPAN, HEAD_DIM), lambda j, n: (j, n, 0))
    return pl.pallas_call(
        body, name=f"attn_fwd_g{group}", grid=(HEADS_PER_GROUP // hp, nb),
        in_specs=in_specs, out_specs=[out] * 2,
        out_shape=[jax.ShapeDtypeStruct((HEADS_PER_GROUP, S, HEAD_DIM), F32)] * 2,
        compiler_params=_params(("parallel", "parallel"), VMEM_LIMIT),
    )(*([proj] * (len(in_specs) - 1)), bias)


_PROJ_PART = dict(q=0, qn=0, kp=1, kc=1, vp=2, vc=2)


def _head_spec(r, nb, hp, kind, group, jj):
    if kind in _PROJ_PART:
        base = (_PROJ_PART[kind] * N_GROUPS + group) * HEADS_PER_GROUP
    else:
        base = 0
    if kind.endswith("p"):
        row = lambda n: jnp.maximum(n - 1, 0)
    elif kind.endswith("n"):
        row = lambda n: jnp.minimum(n + 1, nb - 1)
    else:
        row = lambda n: n
    return pl.BlockSpec((r * SPAN, HEAD_DIM), lambda j, n: (row(n), base + j * hp + jj))


def _attn_merge_call(parts):
    S = parts[0].shape[1]

    def body(o1, s1, o2, s2, o3, s3, a_ref, ab_ref, lse_ref):
        for j in range(HEADS_PER_GROUP):
            sl = slice(j * HEAD_DIM, (j + 1) * HEAD_DIM)
            mx = jnp.maximum(jnp.maximum(s1[j], s2[j]), s3[j])
            w1 = jnp.exp(s1[j] - mx)
            w2 = jnp.exp(s2[j] - mx)
            w3 = jnp.exp(s3[j] - mx)
            den = w1 + w2 + w3
            a = (w1 * o1[j] + w2 * o2[j] + w3 * o3[j]) / den
            a_ref[:, sl] = a
            ab_ref[:, sl] = a.astype(BF16)
            lse_ref[:, sl] = mx + jnp.log(den)

    heads = pl.BlockSpec((HEADS_PER_GROUP, ROW_TILE, HEAD_DIM), lambda i: (0, i, 0))
    return pl.pallas_call(
        body, name="attn_merge", grid=(S // ROW_TILE,),
        in_specs=[heads] * 6, out_specs=[_row_spec(GROUP_WIDTH)] * 3,
        out_shape=[jax.ShapeDtypeStruct((S, GROUP_WIDTH), F32), jax.ShapeDtypeStruct((S, GROUP_WIDTH), BF16),
                   jax.ShapeDtypeStruct((S, GROUP_WIDTH), F32)],
        compiler_params=_params(("parallel",)),
    )(*parts)


def _attn_delta_call(a, da):
    S = a.shape[0]

    def body(a_ref, da_ref, d_ref):
        for j in range(HEADS_PER_GROUP):
            sl = slice(j * HEAD_DIM, (j + 1) * HEAD_DIM)
            d = jnp.sum(a_ref[:, sl] * da_ref[:, sl], axis=-1, keepdims=True)
            d_ref[:, sl] = jnp.broadcast_to(d, (ROW_TILE, HEAD_DIM))

    return pl.pallas_call(
        body, name="attn_delta", grid=(S // ROW_TILE,),
        in_specs=[_row_spec(GROUP_WIDTH)] * 2, out_specs=_row_spec(GROUP_WIDTH),
        out_shape=jax.ShapeDtypeStruct((S, GROUP_WIDTH), F32),
        compiler_params=_params(("parallel",)),
    )(a, da)


def _attn_bwd_call(proj, bias, da, lse, delta, group):
    S = proj.shape[0]
    r, nb, hp, per = _attn_plan(S, group)
    scale = HEAD_DIM ** -0.5
    kinds = ("q", "qn", "kp", "kc", "vp", "vc", "da", "dan", "lse", "lsen", "dl", "dln") if nb > 1 else ("q", "kc", "vc", "da", "lse", "dl")
    source = dict(da=da, dan=da, lse=lse, lsen=lse, dl=delta, dln=delta)

    def body(*refs):
        ins = {kind: refs[i * hp:(i + 1) * hp] for i, kind in enumerate(kinds)}
        b_ref, dq_ref, dk_ref, dv_ref, db_ref = refs[len(kinds) * hp:]
        n = pl.program_id(1)
        prev_ok, cur_ok, next_ok = _band_masks(n, nb)

        @pl.when(n == 0)
        def _():
            db_ref[...] = jnp.zeros_like(db_ref)

        band_ok = _wide_band_mask(n) if nb > 1 else cur_ok

        def residue(rho):
            rows = _residue_rows(rho, r)
            for j in range(hp):
                get = lambda kind: ins[kind][j][rows, :]
                q = get("q").astype(BF16)
                kc = get("kc").astype(BF16)
                vc = get("vc").astype(BF16)
                dav = get("da").astype(BF16)
                lse_q, dl_q = get("lse"), get("dl")
                if nb == 1:
                    pc = jnp.exp(jnp.where(cur_ok, _dot_nt(q, kc) * scale + b_ref[j, :, SPAN:], NEG_INF) - lse_q)
                    dsc = pc * (_dot_nt(dav, vc) - dl_q)
                    dsc_b = dsc.astype(BF16)
                    dq = _dot_nn(dsc_b, kc)
                    dk = _dot_tn(dsc_b, q)
                    dv = _dot_tn(pc.astype(BF16), dav)
                    db_ref[j, :, SPAN:] += dsc
                else:
                    qn = get("qn").astype(BF16)
                    dan = get("dan").astype(BF16)
                    keys = jnp.concatenate([get("kp").astype(BF16), kc], axis=0)
                    vals = jnp.concatenate([get("vp").astype(BF16), vc], axis=0)
                    wide = lambda t: jnp.concatenate([t, t], axis=1)
                    p = jnp.exp(jnp.where(band_ok, _dot_nt(q, keys) * scale + b_ref[j], NEG_INF) - wide(lse_q))
                    ds = p * (_dot_nt(dav, vals) - wide(dl_q))
                    dq = _dot_nn(ds.astype(BF16), keys)
                    db_ref[j] += ds
                    pn = jnp.exp(jnp.where(next_ok, _dot_nt(qn, kc) * scale + b_ref[j, :, :SPAN], NEG_INF) - get("lsen"))
                    dsn = pn * (_dot_nt(dan, vc) - get("dln"))
                    both = lambda cur_part, next_part: jnp.concatenate([cur_part.astype(BF16), next_part.astype(BF16)], axis=0)
                    dk = _dot_tn(both(ds[:, SPAN:], dsn), jnp.concatenate([q, qn], axis=0))
                    dv = _dot_tn(both(p[:, SPAN:], pn), jnp.concatenate([dav, dan], axis=0))
                dq_ref[j, rows, :] = dq * scale
                dk_ref[j, rows, :] = dk * scale
                dv_ref[j, rows, :] = dv

        _for_residues(r, per, residue)

    per_group = HEADS_PER_GROUP // hp
    band = (hp, SPAN, 2 * SPAN)
    in_specs = [_head_spec(r, nb, hp, kind, group, jj) for kind in kinds for jj in range(hp)]
    in_specs.append(pl.BlockSpec(band, lambda j, n: (group * per_group + j, 0, 0)))
    operands = [source.get(kind, proj) for kind in kinds for _ in range(hp)] + [bias]
    out = pl.BlockSpec((hp, r * SPAN, HEAD_DIM), lambda j, n: (j, n, 0))
    return pl.pallas_call(
        body, name=f"attn_bwd_g{group}", grid=(per_group, nb),
        in_specs=in_specs,
        out_specs=[out] * 3 + [pl.BlockSpec(band, lambda j, n: (j, 0, 0))],
        out_shape=[jax.ShapeDtypeStruct((HEADS_PER_GROUP, S, HEAD_DIM), F32)] * 3
        + [jax.ShapeDtypeStruct((HEADS_PER_GROUP, SPAN, 2 * SPAN), F32)],
        compiler_params=_params(("parallel", "arbitrary"), VMEM_LIMIT),
    )(*operands)


def _dproj_call(dqkv, tails):
    S = tails[0].shape[0]
    width = len(dqkv) * GROUP_WIDTH + sum(t.shape[1] for t in tails)

    def body(*refs):
        o_ref = refs[-1]
        col = 0
        for ref in refs[:len(dqkv)]:
            for j in range(HEADS_PER_GROUP):
                o_ref[:, col:col + HEAD_DIM] = ref[j].astype(BF16)
                col += HEAD_DIM
        for ref in refs[len(dqkv):-1]:
            o_ref[:, col:col + ref.shape[1]] = ref[...]
            col += ref.shape[1]

    heads = pl.BlockSpec((HEADS_PER_GROUP, ROW_TILE, HEAD_DIM), lambda i: (0, i, 0))
    return pl.pallas_call(
        body, name="dproj_assemble", grid=(S // ROW_TILE,),
        in_specs=[heads] * len(dqkv) + [_row_spec(t.shape[1]) for t in tails],
        out_specs=_row_spec(width), out_shape=jax.ShapeDtypeStruct((S, width), BF16),
        compiler_params=_params(("parallel",)),
    )(*dqkv, *tails)


def _tap_rows(xpad_ref, t0, k, width, pad):
    return xpad_ref[pl.ds(t0 + (pad - (width - 1 - k)), TIME_BLOCK), :]


def _conv_block(xpad_ref, t0, w_ref, width, pad):
    acc = None
    for k in range(width):
        term = w_ref[k:k + 1, :] * _tap_rows(xpad_ref, t0, k, width, pad)
        acc = term if acc is None else acc + term
    return acc


def _conv_transpose_block(dpad_ref, t0, w_ref, width):
    acc = None
    for k in range(width):
        term = w_ref[k:k + 1, :] * dpad_ref[pl.ds(t0 + (width - 1 - k), TIME_BLOCK), :]
        acc = term if acc is None else acc + term
    return acc


def _conv_weight_grad(xpad_ref, t0, dy, dw_ref, width, pad):
    for k in range(width):
        dw_ref[k:k + 1, :] += jnp.sum(dy * _tap_rows(xpad_ref, t0, k, width, pad), axis=0, keepdims=True)


def _time_loop(S, step):
    def it(tb, carry):
        step(pl.multiple_of(tb * TIME_BLOCK, TIME_BLOCK))
        return carry

    lax.fori_loop(0, S // TIME_BLOCK, it, 0)


def _conv_fwd_call(proj, col0, w, b):
    S = proj.shape[0]
    C = w.shape[1]
    nt = C // LANES
    v0, g0 = col0 // LANES, (col0 + C) // LANES

    def body(val_ref, gate_ref, w_ref, b_ref, o_ref, pad_ref):
        pad_ref[0:CONV_PAD, :] = jnp.zeros((CONV_PAD, LANES), F32)
        pad_ref[CONV_PAD:, :] = val_ref[...] * _sigmoid(gate_ref[...])

        def step(t0):
            o_ref[pl.ds(t0, TIME_BLOCK), :] = _conv_block(pad_ref, t0, w_ref, CONV_WIDTH, CONV_PAD) + b_ref[...]

        _time_loop(S, step)

    seq = lambda off: pl.BlockSpec((S, LANES), lambda i: (0, off + i))
    return pl.pallas_call(
        body, name="conv_module", grid=(nt,),
        in_specs=[seq(v0), seq(g0), pl.BlockSpec((CONV_WIDTH, LANES), lambda i: (0, i)), pl.BlockSpec((1, LANES), lambda i: (0, i))],
        out_specs=seq(0), out_shape=jax.ShapeDtypeStruct((S, C), F32),
        scratch_shapes=[pltpu.VMEM((S + CONV_PAD, LANES), F32)],
        compiler_params=_params(("parallel",)),
    )(proj, proj, w, b)


def _conv_bwd_call(proj, col0, w, dc1):
    S = proj.shape[0]
    C = w.shape[1]
    nt = C // LANES
    v0, g0 = col0 // LANES, (col0 + C) // LANES

    def body(val_ref, gate_ref, w_ref, dy_ref, dval_ref, dgate_ref, dw_ref, db_ref, xpad_ref, dpad_ref, dwacc_ref):
        xpad_ref[0:CONV_PAD, :] = jnp.zeros((CONV_PAD, LANES), F32)
        xpad_ref[CONV_PAD:, :] = val_ref[...] * _sigmoid(gate_ref[...])
        dpad_ref[0:S, :] = dy_ref[...]
        dpad_ref[S:, :] = jnp.zeros((CONV_PAD, LANES), F32)
        dwacc_ref[...] = jnp.zeros_like(dwacc_ref)

        def step(t0):
            rows = pl.ds(t0, TIME_BLOCK)
            _conv_weight_grad(xpad_ref, t0, dy_ref[rows, :], dwacc_ref, CONV_WIDTH, CONV_PAD)
            dc0 = _conv_transpose_block(dpad_ref, t0, w_ref, CONV_WIDTH)
            sg = _sigmoid(gate_ref[rows, :])
            dval_ref[rows, :] = (dc0 * sg).astype(BF16)
            dgate_ref[rows, :] = (dc0 * val_ref[rows, :] * sg * (1.0 - sg)).astype(BF16)

        _time_loop(S, step)
        dw_ref[...] = dwacc_ref[...]
        db_ref[...] = jnp.sum(dy_ref[...], axis=0, keepdims=True)

    seq = lambda off: pl.BlockSpec((S, LANES), lambda i: (0, off + i))
    return pl.pallas_call(
        body, name="conv_module_bwd", grid=(nt,),
        in_specs=[seq(v0), seq(g0), pl.BlockSpec((CONV_WIDTH, LANES), lambda i: (0, i)), seq(0)],
        out_specs=[seq(0), seq(0), pl.BlockSpec((CONV_PAD, LANES), lambda i: (0, i)), pl.BlockSpec((1, LANES), lambda i: (0, i))],
        out_shape=[jax.ShapeDtypeStruct((S, C), BF16), jax.ShapeDtypeStruct((S, C), BF16),
                   jax.ShapeDtypeStruct((CONV_PAD, C), F32), jax.ShapeDtypeStruct((1, C), F32)],
        scratch_shapes=[pltpu.VMEM((S + CONV_PAD, LANES), F32), pltpu.VMEM((S + CONV_PAD, LANES), F32),
                        pltpu.VMEM((CONV_PAD, LANES), F32)],
        compiler_params=_params(("parallel",)),
    )(proj, proj, w, dc1)


def _ffn_fwd_call(u, w, b):
    S, C2 = u.shape
    C = C2 // 2
    nt = C // LANES

    def body(ug_ref, uv_ref, wg_ref, wv_ref, bg_ref, bv_ref, f_ref, pg_ref, pv_ref):
        zeros = jnp.zeros((FFN_PAD, LANES), F32)
        pg_ref[0:FFN_PAD, :] = zeros
        pv_ref[0:FFN_PAD, :] = zeros
        pg_ref[FFN_PAD:, :] = ug_ref[...]
        pv_ref[FFN_PAD:, :] = uv_ref[...]

        def step(t0):
            cg = _conv_block(pg_ref, t0, wg_ref, FFN_CONV_WIDTH, FFN_PAD) + bg_ref[...]
            cv = _conv_block(pv_ref, t0, wv_ref, FFN_CONV_WIDTH, FFN_PAD) + bv_ref[...]
            f_ref[pl.ds(t0, TIME_BLOCK), :] = (_gelu(cg) * cv).astype(BF16)

        _time_loop(S, step)

    seq = lambda off: pl.BlockSpec((S, LANES), lambda i: (0, off + i))
    wsp = lambda off: pl.BlockSpec((FFN_CONV_WIDTH, LANES), lambda i: (0, off + i))
    bsp = lambda off: pl.BlockSpec((1, LANES), lambda i: (0, off + i))
    return pl.pallas_call(
        body, name="ffn_conv_geglu", grid=(nt,),
        in_specs=[seq(0), seq(nt), wsp(0), wsp(nt), bsp(0), bsp(nt)],
        out_specs=seq(0), out_shape=jax.ShapeDtypeStruct((S, C), BF16),
        scratch_shapes=[pltpu.VMEM((S + FFN_PAD, LANES), F32)] * 2,
        compiler_params=_params(("parallel",)),
    )(u, u, w, w, b, b)


def _ffn_bwd_call(u, w, b, df):
    S, C2 = u.shape
    C = C2 // 2
    nt = C // LANES

    def body(ug_ref, uv_ref, wg_ref, wv_ref, bg_ref, bv_ref, df_ref,
             du_ref, dwg_ref, dwv_ref, dbg_ref, dbv_ref,
             pg_ref, pv_ref, dg_ref, dv_ref, dwg_acc, dwv_acc, dbg_acc, dbv_acc):
        zeros = jnp.zeros((FFN_PAD, LANES), F32)
        pg_ref[0:FFN_PAD, :] = zeros
        pv_ref[0:FFN_PAD, :] = zeros
        pg_ref[FFN_PAD:, :] = ug_ref[...]
        pv_ref[FFN_PAD:, :] = uv_ref[...]
        dg_ref[S:, :] = zeros
        dv_ref[S:, :] = zeros
        dwg_acc[...] = jnp.zeros_like(dwg_acc)
        dwv_acc[...] = jnp.zeros_like(dwv_acc)
        dbg_acc[...] = jnp.zeros_like(dbg_acc)
        dbv_acc[...] = jnp.zeros_like(dbv_acc)

        def first(t0):
            rows = pl.ds(t0, TIME_BLOCK)
            cg = _conv_block(pg_ref, t0, wg_ref, FFN_CONV_WIDTH, FFN_PAD) + bg_ref[...]
            cv = _conv_block(pv_ref, t0, wv_ref, FFN_CONV_WIDTH, FFN_PAD) + bv_ref[...]
            dfb = df_ref[rows, :]
            gelu, gelu_grad = _gelu_and_grad(cg)
            dcg = dfb * cv * gelu_grad
            dcv = dfb * gelu
            dg_ref[rows, :] = dcg
            dv_ref[rows, :] = dcv
            _conv_weight_grad(pg_ref, t0, dcg, dwg_acc, FFN_CONV_WIDTH, FFN_PAD)
            _conv_weight_grad(pv_ref, t0, dcv, dwv_acc, FFN_CONV_WIDTH, FFN_PAD)
            dbg_acc[...] += jnp.sum(dcg, axis=0, keepdims=True)
            dbv_acc[...] += jnp.sum(dcv, axis=0, keepdims=True)

        def second(t0):
            rows = pl.ds(t0, TIME_BLOCK)
            du_ref[0, rows, :] = _conv_transpose_block(dg_ref, t0, wg_ref, FFN_CONV_WIDTH).astype(BF16)
            du_ref[1, rows, :] = _conv_transpose_block(dv_ref, t0, wv_ref, FFN_CONV_WIDTH).astype(BF16)

        _time_loop(S, first)
        _time_loop(S, second)
        dwg_ref[...] = dwg_acc[...]
        dwv_ref[...] = dwv_acc[...]
        dbg_ref[...] = dbg_acc[...]
        dbv_ref[...] = dbv_acc[...]

    seq = lambda off: pl.BlockSpec((S, LANES), lambda i: (0, off + i))
    wsp = lambda off: pl.BlockSpec((FFN_CONV_WIDTH, LANES), lambda i: (0, off + i))
    bsp = lambda off: pl.BlockSpec((1, LANES), lambda i: (0, off + i))
    return pl.pallas_call(
        body, name="ffn_conv_geglu_bwd", grid=(nt,),
        in_specs=[seq(0), seq(nt), wsp(0), wsp(nt), bsp(0), bsp(nt), seq(0)],
        out_specs=[pl.BlockSpec((2, S, LANES), lambda i: (0, 0, i)),
                   pl.BlockSpec((SUBLANES, LANES), lambda i: (0, i)), pl.BlockSpec((SUBLANES, LANES), lambda i: (0, i)),
                   bsp(0), bsp(0)],
        out_shape=[jax.ShapeDtypeStruct((2, S, C), BF16)] + [jax.ShapeDtypeStruct((SUBLANES, C), F32)] * 2
        + [jax.ShapeDtypeStruct((1, C), F32)] * 2,
        scratch_shapes=[pltpu.VMEM((S + FFN_PAD, LANES), F32)] * 4 + [pltpu.VMEM((SUBLANES, LANES), F32)] * 2
        + [pltpu.VMEM((1, LANES), F32)] * 2,
        compiler_params=_params(("parallel",)),
    )(u, u, w, w, b, b, df)


def _adamw(w_ref, g_ref, m_ref, v_ref, d_ref, mo_ref, vo_ref):
    gv = g_ref[...]
    mn = ADAM_B1 * m_ref[...] + (1.0 - ADAM_B1) * gv
    vn = ADAM_B2 * v_ref[...] + (1.0 - ADAM_B2) * (gv * gv)
    mo_ref[...] = mn
    vo_ref[...] = vn
    m_hat = mn * (1.0 / (1.0 - ADAM_B1 ** ADAM_STEP))
    v_hat = vn * (1.0 / (1.0 - ADAM_B2 ** ADAM_STEP))
    d_ref[...] = -ADAM_LR * (m_hat / (jnp.sqrt(v_hat) + ADAM_EPS) + ADAM_WD * w_ref[...])


def _adamw_call(w, g, m, v, name):
    R, C = w.shape
    tr = _row_tile(R, C)

    def body(w_ref, g_ref, m_ref, v_ref, go_ref, d_ref, mo_ref, vo_ref):
        go_ref[...] = g_ref[...]
        _adamw(w_ref, g_ref, m_ref, v_ref, d_ref, mo_ref, vo_ref)

    spec = pl.BlockSpec((tr, C), lambda i: (i, 0))
    return pl.pallas_call(
        body, name=name, grid=(R // tr,),
        in_specs=[spec] * 4, out_specs=[spec] * 4,
        out_shape=[jax.ShapeDtypeStruct((R, C), F32)] * 4,
        compiler_params=_params(("parallel",)),
    )(w, g, m, v)


def _adamw_small_call(ws, gs, ms, vs):
    n = len(ws)

    def body(*refs):
        w_refs, g_refs, m_refs, v_refs, d_refs, mo_refs, vo_refs = (refs[i * n:(i + 1) * n] for i in range(7))
        for i in range(n):
            _adamw(w_refs[i], g_refs[i], m_refs[i], v_refs[i], d_refs[i], mo_refs[i], vo_refs[i])

    whole = pl.BlockSpec(memory_space=pltpu.VMEM)
    outs = pl.pallas_call(
        body, name="adamw_small",
        in_specs=[whole] * (4 * n), out_specs=[whole] * (3 * n),
        out_shape=[jax.ShapeDtypeStruct(w.shape, F32) for w in ws] * 3,
    )(*ws, *gs, *ms, *vs)
    return outs[:n], outs[n:2 * n], outs[2 * n:]


def _position():
    return lax.axis_index("x"), lax.axis_index("y"), lax.axis_index("c")


def _chip_peers(x, y):
    return [(x, 1 - y), (1 - x, y), (1 - x, 1 - y)]


def _half_rows(ref, core, rows):
    h = rows // 2
    start = pl.multiple_of(core * h, 16)
    return ref.at[pl.ds(start, h), :] if len(ref.shape) == 2 else ref.at[:, pl.ds(start, h), :]


def _shard_half(ref, shard, core, rows):
    h = rows // 2
    return ref.at[shard, pl.ds(pl.multiple_of(core * h, 16), h), :]


ANY = pl.BlockSpec(memory_space=pl.ANY)


def _first_hop_copies(srcs, lands):
    x, y, c = _position()
    chip = 2 * x + y
    targets = [(px, py, c) for px, py in _chip_peers(x, y)] + [(x, y, 1 - c)]
    rows = srcs[0].shape[0]
    out = []
    for i, (s, l) in enumerate(zip(srcs, lands)):
        for k, dev in enumerate(targets):
            if i == 0 and k < 3:
                out.append((_half_rows(s, c, rows), _shard_half(l, chip, c, rows), dev, k))
            else:
                out.append((s, l.at[chip], dev, len(targets) * i + k))
    return out


def _second_hop_copies(srcs, lands):
    x, y, c = _position()
    rows = lands[0].shape[1]
    out = []
    for k, (px, py) in enumerate(_chip_peers(x, y)):
        half = _shard_half(lands[0], 2 * px + py, c, rows)
        out.append((half, half, (x, y, 1 - c), k))
    return out


HBM_SPEC = pl.BlockSpec(memory_space=pltpu.HBM)
SEM_SPEC = pl.BlockSpec(memory_space=pltpu.SEMAPHORE)
DATAFLOW = pltpu.SideEffectType.DATAFLOW_SIDE_EFFECTING


def _in_hbm(a):
    return pltpu.with_memory_space_constraint(a, pltpu.HBM)


def _split_start(name, groups, after, carry=None):
    spans, arrays = [], []
    for srcs, lands, _, _ in groups:
        spans.append((len(arrays), len(srcs), len(lands)))
        arrays += list(srcs) + list(lands)
    if carry is not None:
        arrays.append(carry)
    na, ng = len(arrays), len(groups)

    def body(*refs):
        sems, token = refs[na + 1:na + 1 + 2 * ng], refs[-1]
        for g, (_, _, _, copies) in enumerate(groups):
            off, ns, nl = spans[g]
            for src, dst, dev, idx in copies(refs[off:off + ns], refs[off + ns:off + ns + nl]):
                pltpu.make_async_remote_copy(src_ref=src, dst_ref=dst, send_sem=sems[2 * g].at[idx], recv_sem=sems[2 * g + 1].at[idx],
                                             device_id=dev, device_id_type=MESH).start()
        token[...] = jnp.zeros_like(token)

    outs = pl.pallas_call(
        body, name=name,
        in_specs=[HBM_SPEC] * na + [ANY],
        out_specs=[SEM_SPEC] * (2 * ng) + [HBM_SPEC] * na + [pl.BlockSpec(memory_space=pltpu.VMEM)],
        out_shape=[pltpu.SemaphoreType.DMA((n_sems,)) for _, _, n_sems, _ in groups for _ in range(2)]
        + [pltpu.HBM(a.shape, a.dtype) for a in arrays] + [jax.ShapeDtypeStruct((SUBLANES, LANES), F32)],
        input_output_aliases={i: 2 * ng + i for i in range(na)},
        compiler_params=pltpu.CompilerParams(has_side_effects=DATAFLOW),
    )(*[_in_hbm(a) for a in arrays], after)
    started = []
    for g, (off, ns, nl) in enumerate(spans):
        thru = outs[2 * ng + off:2 * ng + off + ns + nl]
        started.append(dict(send=outs[2 * g], recv=outs[2 * g + 1], srcs=list(thru[:ns]), lands=list(thru[ns:]),
                            tile=outs[-1], token=outs[-1][0, 0], carry=None if carry is None else outs[2 * ng + na - 1]))
    return started


def _split_wait(name, started, copies, after):
    n, m = len(started["srcs"]), len(started["lands"])

    def body(*refs):
        src_refs, land_refs = refs[:n], refs[n:n + m]
        send_sem, recv_sem = refs[n + m], refs[n + m + 1]
        for src, dst, dev, idx in copies(src_refs, land_refs):
            cp = pltpu.make_async_remote_copy(src_ref=src, dst_ref=dst, send_sem=send_sem.at[idx], recv_sem=recv_sem.at[idx],
                                              device_id=dev, device_id_type=MESH)
            cp.wait_send()
            cp.wait_recv()

    arrays = started["srcs"] + started["lands"]
    outs = pl.pallas_call(
        body, name=name,
        in_specs=[HBM_SPEC] * (n + m) + [SEM_SPEC, SEM_SPEC, ANY],
        out_specs=[HBM_SPEC] * (n + m),
        out_shape=[pltpu.HBM(a.shape, a.dtype) for a in arrays],
        input_output_aliases={i: i for i in range(n + m)},
        compiler_params=pltpu.CompilerParams(has_side_effects=DATAFLOW),
    )(*arrays, started["send"], started["recv"], after)
    return list(outs)


def _gather_copies(srcs, lands):
    x, y, c = _position()
    chip = 2 * x + y
    targets = [(px, py, c) for px, py in _chip_peers(x, y)] + [(x, y, 1 - c)]
    return [(s, l.at[chip], dev, len(targets) * i + k) for i, (s, l) in enumerate(zip(srcs, lands)) for k, dev in enumerate(targets)]


def _sibling_copies(srcs, lands):
    x, y, c = _position()
    return [(_half_rows(srcs[0], 1 - c, srcs[0].shape[1]), lands[0], (x, y, 1 - c), 0)]


def _exchange_copies(srcs, lands):
    x, y, c = _position()
    return [(srcs[0].at[2 * px + py], lands[0].at[k], (px, py, c), k) for k, (px, py) in enumerate(_chip_peers(x, y))]


def _pair_sum_call(grad, recv, core, name):
    _, h, B = recv.shape
    tr = _row_tile(h, B)

    def body(core_ref, g_ref, r_ref, o_ref, ob_ref):
        s = g_ref[...] + r_ref[...]
        o_ref[...] = s
        ob_ref[...] = s.astype(BF16)

    g_spec = pl.BlockSpec((None, tr, B), lambda q, i, core_ref: (q, core_ref[0] * (h // tr) + i, 0))
    spec = pl.BlockSpec((None, tr, B), lambda q, i, core_ref: (q, i, 0))
    return pl.pallas_call(
        body, name=name,
        grid_spec=pltpu.PrefetchScalarGridSpec(num_scalar_prefetch=1, grid=(N_CHIPS, h // tr), in_specs=[g_spec, spec],
                                               out_specs=[spec, spec]),
        out_shape=[jax.ShapeDtypeStruct(recv.shape, F32), jax.ShapeDtypeStruct(recv.shape, BF16)],
        compiler_params=_params(("parallel", "parallel")),
    )(core, grad, recv)


def _chip_sum_call(partial, recv, chip_core, name):
    _, h, B = recv.shape
    tr = _row_tile(h, B)

    def body(cc_ref, p_ref, r_ref, o_ref):
        o_ref[...] = ((p_ref[...] + r_ref[0].astype(F32)) + r_ref[1].astype(F32)) + r_ref[2].astype(F32)

    return pl.pallas_call(
        body, name=name,
        grid_spec=pltpu.PrefetchScalarGridSpec(
            num_scalar_prefetch=1, grid=(h // tr,),
            in_specs=[pl.BlockSpec((None, tr, B), lambda i, cc_ref: (cc_ref[0], i, 0)),
                      pl.BlockSpec((3, tr, B), lambda i, cc_ref: (0, i, 0))],
            out_specs=pl.BlockSpec((tr, B), lambda i, cc_ref: (cc_ref[1] * (h // tr) + i, 0))),
        out_shape=jax.ShapeDtypeStruct((2 * h, B), F32),
        compiler_params=_params(("parallel",)),
    )(chip_core, partial, recv)


def _sibling_assemble_call(shards, name="grad_sibling_assemble"):
    n = len(shards)

    def body(*refs):
        ins, outs = refs[:n], refs[n:2 * n]
        send_sems, recv_sems = refs[2 * n:]
        x, y, c = _position()
        copies = []
        for i in range(n):
            rows = shards[i].shape[0]
            cp = pltpu.make_async_remote_copy(src_ref=_half_rows(ins[i], c, rows), dst_ref=_half_rows(outs[i], c, rows),
                                              send_sem=send_sems.at[i], recv_sem=recv_sems.at[i],
                                              device_id=(x, y, 1 - c), device_id_type=MESH)
            cp.start()
            copies.append(cp)
        for cp in copies:
            cp.wait()

    return pl.pallas_call(
        body, name=name,
        in_specs=[ANY] * n, out_specs=[ANY] * n,
        out_shape=[jax.ShapeDtypeStruct(s.shape, F32) for s in shards],
        input_output_aliases={i: i for i in range(n)},
        scratch_shapes=[pltpu.SemaphoreType.DMA((n,)), pltpu.SemaphoreType.DMA((n,))],
    )(*shards)


N_DEVICES = 8


def _allsum_copies(srcs, lands):
    x, y, c = _position()
    me = 4 * x + 2 * y + c
    out = []
    for k in range(1, N_DEVICES):
        peer = (1 - x if k & 4 else x, 1 - y if k & 2 else y, 1 - c if k & 1 else c)
        out.append((srcs[0], lands[0].at[me], peer, k - 1))
    return out


def _ordered_sum_call(mine, landed, me):
    rows = mine.shape[0]

    def body(me_ref, x_ref, l_ref, o_ref):
        acc = jnp.where(me_ref[0] == 0, x_ref[...], l_ref[0])
        for d in range(1, N_DEVICES):
            acc = acc + jnp.where(me_ref[0] == d, x_ref[...], l_ref[d])
        o_ref[...] = acc

    return pl.pallas_call(
        body, name="small_grad_sum",
        in_specs=[pl.BlockSpec(memory_space=pltpu.SMEM), pl.BlockSpec(memory_space=pltpu.VMEM), pl.BlockSpec(memory_space=pltpu.VMEM)],
        out_specs=pl.BlockSpec(memory_space=pltpu.VMEM),
        out_shape=jax.ShapeDtypeStruct((rows, LANES), F32),
    )(me, mine, landed)


def _pack(arrays):
    flat = jnp.concatenate([a.reshape(-1).astype(F32) for a in arrays])
    rows = -(-flat.shape[0] // LANES)
    rows = -(-rows // SUBLANES) * SUBLANES
    flat = jnp.pad(flat, (0, rows * LANES - flat.shape[0]))
    return flat.reshape(rows, LANES)


def _unpack(packed, shapes):
    flat = packed.reshape(-1)
    out, off = [], 0
    for shp in shapes:
        size = int(np.prod(shp))
        out.append(flat[off:off + size].reshape(shp))
        off += size
    return out


def _local_step(xs, target, P, late_weights, on_grad):
    S, D = xs.shape
    qkv_width = 3 * N_HEADS * HEAD_DIM
    glu_col0, gate_col0 = qkv_width, qkv_width + 2 * D
    shard_major = lambda g: g.reshape(N_CHIPS, g.shape[0] // N_CHIPS, g.shape[1])

    h1 = _rms_fwd_call(xs, P["norm_mix_pre"])
    buckets = _bucket_tables()
    bias = _bias_table_call(P["rel_bias"] + 0.0 * h1[0, 0].astype(F32), buckets)
    P = dict(P, **late_weights("in", bias))
    proj = _matmul(h1, P["w_in"], "nn", "proj_in")
    parts = []
    for g in range(N_GROUPS):
        parts += _attn_fwd_call(proj, bias, g)
    a, a_bf, lse = _attn_merge_call(parts)
    P = dict(P, **late_weights("mix", a_bf))
    y_a = _matmul(a_bf, P["w_attn_out"], "nn", "attn_out")
    c1 = _conv_fwd_call(proj, glu_col0, P["conv_dw_w"], P["conv_dw_b"])
    cact = _ln_silu_call(c1, P["conv_ln_g"], P["conv_ln_b"])
    y_c = _matmul(cact, P["conv_pw_w"], "nn", "conv_pw")
    mixed = _mix_call(proj, gate_col0, P["b_gate"], y_a, y_c)
    out = _matmul(mixed, P["w_out"], "nn", "mix_out")
    x1, h2 = _res1_call(xs, out, P["norm_mix_post"], P["norm_ffn_pre"])
    P = dict(P, **late_weights("ffn", h2))
    u = _matmul(h2, P["w_up"], "nn", "ffn_up")
    f = _ffn_fwd_call(u, P["ffn_conv_w"], P["ffn_conv_b"])
    yff = _matmul(f, P["w_down"], "nn", "ffn_down")
    loss_tile, dx2, dyff, dg_ffn_post = _loss_call(yff, x1, P["norm_ffn_post"], target)

    G = {}
    G["norm_ffn_post"] = dg_ffn_post
    zero = on_grad("w_down", shard_major(_matmul(f, dyff, "tn", "ffn_down_dw")))
    df = _matmul(dyff, P["w_down"], "nt", "ffn_down_dx")
    du, dwg, dwv, dbg, dbv = _ffn_bwd_call(u, P["ffn_conv_w"], P["ffn_conv_b"] + zero, df)
    G["ffn_conv_w"] = jnp.concatenate([dwg[:FFN_CONV_WIDTH], dwv[:FFN_CONV_WIDTH]], axis=1)
    G["ffn_conv_b"] = jnp.concatenate([dbg, dbv], axis=1)
    zero = on_grad("w_up", _matmul(h2, du, "tn", "ffn_up_dw", out_shards=True))
    dh2 = _matmul(du, P["w_up"], "nt", "ffn_up_dx")
    dx1, dout, G["norm_ffn_pre"], G["norm_mix_post"] = _mid_bwd_call(x1, P["norm_ffn_pre"] + zero, dh2, dx2, out, P["norm_mix_post"])
    zero = on_grad("w_out", shard_major(_matmul(mixed, dout, "tn", "mix_out_dw")))
    dmixed = _matmul(dout, P["w_out"], "nt", "mix_out_dx")
    dya, dyc, dga, dgc, dba, dbc = _mix_bwd_call(dmixed, proj, gate_col0, P["b_gate"] + zero, y_a, y_c)
    G["b_gate"] = jnp.concatenate([dba, dbc], axis=1)
    zero = on_grad("w_attn_out", _matmul(a_bf, dya, "tn", "attn_out_dw", out_shards=True))
    zero = zero + on_grad("conv_pw_w", shard_major(_matmul(cact, dyc, "tn", "conv_pw_dw")))
    da = _matmul(dya, P["w_attn_out"], "nt", "attn_out_dx")
    dcact = _matmul(dyc, P["conv_pw_w"], "nt", "conv_pw_dx")
    dc1, G["conv_ln_g"], G["conv_ln_b"] = _ln_silu_bwd_call(c1, P["conv_ln_g"] + zero, P["conv_ln_b"], dcact)
    dval, dgate, dw_dw, G["conv_dw_b"] = _conv_bwd_call(proj, glu_col0, P["conv_dw_w"], dc1)
    G["conv_dw_w"] = dw_dw[:CONV_WIDTH]
    delta = _attn_delta_call(a, da)
    dqs, dks, dvs, dbs = [], [], [], []
    for g in range(N_GROUPS):
        dq, dk, dv, db = _attn_bwd_call(proj, bias, da, lse, delta, g)
        dqs.append(dq)
        dks.append(dk)
        dvs.append(dv)
        dbs.append(db)
    G["rel_bias"] = _bias_grad_call(jnp.concatenate(dbs, axis=0), buckets)
    dproj = _dproj_call(dqs + dks + dvs, [dval, dgate, dga, dgc])
    zero = on_grad("w_in", _matmul(h1, dproj, "tn", "proj_in_dw", out_shards=True))
    dh1 = _matmul(dproj, P["w_in"], "nt", "proj_in_dx")
    zero = zero + on_grad(None, dh1)
    grad_x, G["norm_mix_pre"] = _in_bwd_call(xs, P["norm_mix_pre"] + zero, dh1, dx1)
    return loss_tile, grad_x, G


def kernel(x, w_in, b_gate, rel_bias, w_attn_out, conv_dw_w, conv_dw_b, conv_ln_g, conv_ln_b, conv_pw_w, w_out, norm_mix_pre, norm_mix_post, norm_ffn_pre, norm_ffn_post, w_up, ffn_conv_w, ffn_conv_b, w_down, loss_target, m_w_in, m_b_gate, m_rel_bias, m_w_attn_out, m_conv_dw_w, m_conv_dw_b, m_conv_ln_g, m_conv_ln_b, m_conv_pw_w, m_w_out, m_norm_mix_pre, m_norm_mix_post, m_norm_ffn_pre, m_norm_ffn_post, m_w_up, m_ffn_conv_w, m_ffn_conv_b, m_w_down, v_w_in, v_b_gate, v_rel_bias, v_w_attn_out, v_conv_dw_w, v_conv_dw_b, v_conv_ln_g, v_conv_ln_b, v_conv_pw_w, v_w_out, v_norm_mix_pre, v_norm_mix_post, v_norm_ffn_pre, v_norm_ffn_post, v_w_up, v_ffn_conv_w, v_ffn_conv_b, v_w_down):
    weights = dict(w_in=w_in, b_gate=b_gate, rel_bias=rel_bias, w_attn_out=w_attn_out, conv_dw_w=conv_dw_w, conv_dw_b=conv_dw_b,
                   conv_ln_g=conv_ln_g, conv_ln_b=conv_ln_b, conv_pw_w=conv_pw_w, w_out=w_out, norm_mix_pre=norm_mix_pre,
                   norm_mix_post=norm_mix_post, norm_ffn_pre=norm_ffn_pre, norm_ffn_post=norm_ffn_post, w_up=w_up,
                   ffn_conv_w=ffn_conv_w, ffn_conv_b=ffn_conv_b, w_down=w_down)
    m_in = dict(w_in=m_w_in, b_gate=m_b_gate, rel_bias=m_rel_bias, w_attn_out=m_w_attn_out, conv_dw_w=m_conv_dw_w,
                conv_dw_b=m_conv_dw_b, conv_ln_g=m_conv_ln_g, conv_ln_b=m_conv_ln_b, conv_pw_w=m_conv_pw_w, w_out=m_w_out,
                norm_mix_pre=m_norm_mix_pre, norm_mix_post=m_norm_mix_post, norm_ffn_pre=m_norm_ffn_pre,
                norm_ffn_post=m_norm_ffn_post, w_up=m_w_up, ffn_conv_w=m_ffn_conv_w, ffn_conv_b=m_ffn_conv_b, w_down=m_w_down)
    v_in = dict(w_in=v_w_in, b_gate=v_b_gate, rel_bias=v_rel_bias, w_attn_out=v_w_attn_out, conv_dw_w=v_conv_dw_w,
                conv_dw_b=v_conv_dw_b, conv_ln_g=v_conv_ln_g, conv_ln_b=v_conv_ln_b, conv_pw_w=v_conv_pw_w, w_out=v_w_out,
                norm_mix_pre=v_norm_mix_pre, norm_mix_post=v_norm_mix_post, norm_ffn_pre=v_norm_ffn_pre,
                norm_ffn_post=v_norm_ffn_post, w_up=v_w_up, ffn_conv_w=v_ffn_conv_w, ffn_conv_b=v_ffn_conv_b, w_down=v_w_down)
    names = list(weights)
    xi, yi, ci = _position()
    chip = 2 * xi + yi
    core_arr = jnp.reshape(ci, (1,)).astype(jnp.int32)

    xs = x[0]
    target = loss_target[0]
    S, D = xs.shape

    big = ["w_in", "w_attn_out", "conv_pw_w", "w_out", "w_up", "w_down"]
    row_sharded = ("conv_pw_w", "w_out", "w_down")
    bf16_shard = {k: weights[k][0].astype(BF16) for k in big}
    natural = lambda k, g: g.reshape(-1, g.shape[2]) if k in row_sharded else g
    first_srcs = [bf16_shard["w_in"], conv_dw_w[0], ffn_conv_w[0]]
    first_lands = [lax.empty((N_CHIPS,) + s.shape, s.dtype) for s in first_srcs]
    (first_hop,) = _split_start("gather_in_start", [(first_srcs, first_lands, 4 * len(first_srcs), _first_hop_copies)], core_arr)
    launched = first_hop["token"]
    late_sets = dict(mix=["w_attn_out", "conv_pw_w", "w_out"], ffn=["w_up", "w_down"])
    late_groups = []
    for keys in late_sets.values():
        srcs = [bf16_shard[k] for k in keys]
        late_groups.append((srcs, [lax.empty((N_CHIPS,) + s.shape, BF16) for s in srcs], 4 * len(keys), _gather_copies))
    started = {}

    def late_weights(tag, after):
        if tag == "in":
            w_in_halves, dw4, fc4 = _split_wait("gather_in_wait", first_hop, _first_hop_copies, after)[len(first_srcs):]
            (second_hop,) = _split_start("gather_in_pass_start", [([], [w_in_halves], 3, _second_hop_copies)], dw4)
            (w_in_full,) = _split_wait("gather_in_pass_wait", second_hop, _second_hop_copies, second_hop["tile"])
            started.update(zip(late_sets, _split_start("gather_late_start", late_groups, dw4, carry=w_in_full)))
            return dict(w_in=started["mix"]["carry"], conv_dw_w=jnp.concatenate(list(dw4), axis=1), ffn_conv_w=jnp.concatenate(list(fc4), axis=1))
        landed = _split_wait(f"gather_{tag}_wait", started[tag], _gather_copies, after)[len(late_sets[tag]):]
        return {k: natural(k, g) for k, g in zip(late_sets[tag], landed)}

    chip_core = jnp.stack([chip, ci]).astype(jnp.int32)
    exchanging, pending = {}, {}

    def launch(tag, g3, after):
        keys, groups, partial = [], [], {}
        for k in list(exchanging):
            gk, r1 = _split_wait(f"sibling_exchange_wait_{k}", exchanging.pop(k), _sibling_copies, after)
            partial[k], s16 = _pair_sum_call(gk, r1, core_arr, f"pair_sum_{k}")
            keys.append(k)
            groups.append(([s16], [lax.empty((3,) + s16.shape[1:], BF16)], 3, _exchange_copies))
        if g3 is not None:
            groups.append(([g3], [lax.empty((N_CHIPS, g3.shape[1] // 2, g3.shape[2]), F32)], 1, _sibling_copies))
        begun = _split_start(f"grad_exchange_start_{tag}", groups, core_arr)
        for k, st in zip(keys, begun):
            pending[k] = (partial[k], st)
        if g3 is not None:
            exchanging[tag] = begun[-1]
        return begun[0]["token"]

    def on_grad(k, g3):
        if k is None:
            return launch("last", None, g3[:SUBLANES, :LANES])
        return launch(k, g3, g3[0, :SUBLANES, :LANES])

    def finish(keys, after, tag):
        halves = []
        for k in keys:
            s32, st = pending[k]
            recv2 = _split_wait(f"chip_exchange_wait_{k}", st, _exchange_copies, after)[1]
            halves.append(_chip_sum_call(s32, recv2, chip_core, f"chip_sum_{k}"))
        return dict(zip(keys, _sibling_assemble_call(halves, f"grad_sibling_assemble_{tag}")))

    P = dict(b_gate=b_gate, rel_bias=rel_bias, conv_dw_b=conv_dw_b, conv_ln_g=conv_ln_g, conv_ln_b=conv_ln_b,
             norm_mix_pre=norm_mix_pre + launched, norm_mix_post=norm_mix_post, norm_ffn_pre=norm_ffn_pre,
             norm_ffn_post=norm_ffn_post, ffn_conv_b=ffn_conv_b)
    loss_tile, grad_x, G = _local_step(xs, target, P, late_weights, on_grad)

    small = [k for k in names if k not in big]
    packed = _pack([loss_tile[:1]] + [G[k] for k in small])
    (allsum,) = _split_start("small_grad_allsum_start",
                             [([packed], [jnp.zeros((N_DEVICES,) + packed.shape, F32)], N_DEVICES - 1, _allsum_copies)], core_arr)

    reduced, grads, deltas, new_m, new_v = {}, {}, {}, {}, {}

    def update(keys):
        for k in keys:
            gk, d, mn, vn = _adamw_call(weights[k][0], reduced[k], m_in[k][0], v_in[k][0], f"adamw_{k}")
            grads[k], deltas[k], new_m[k], new_v[k] = gk[None], d[None], mn[None], vn[None]

    others = [k for k in big if k != "w_in"]
    reduced.update(finish(others, allsum["tile"], "others"))
    update(others)
    reduced.update(finish(["w_in"], deltas["w_up"], "w_in"))
    update(["w_in"])

    me = jnp.reshape(4 * xi + 2 * yi + ci, (1,)).astype(jnp.int32)
    mine, landed = _split_wait("small_grad_allsum_wait", allsum, _allsum_copies, deltas["w_in"])
    summed_block = _ordered_sum_call(mine, landed, me)
    loss_row, *summed = _unpack(summed_block, [(1, LANES)] + [G[k].shape for k in small])
    loss = loss_row[0, 0]
    for k, gsum in zip(small, summed):
        if k in ("conv_dw_w", "ffn_conv_w"):
            cols = weights[k].shape[2]
            reduced[k] = lax.dynamic_slice_in_dim(gsum, chip * cols, cols, axis=1)
        else:
            reduced[k] = gsum
    for k in small:
        grads[k] = reduced[k].reshape(weights[k].shape)
    ds, mns, vns = _adamw_small_call([weights[k] for k in small], [grads[k] for k in small],
                                     [m_in[k] for k in small], [v_in[k] for k in small])
    deltas.update(zip(small, ds))
    new_m.update(zip(small, mns))
    new_v.update(zip(small, vns))

    return (loss, grad_x[None], *[grads[k] for k in names], *[deltas[k] for k in names],
            *[new_m[k] for k in names], *[new_v[k] for k in names])
```

```python
import functools
import math

import jax
import jax.numpy as jnp
import numpy as np
from jax import lax
from jax.experimental import pallas as pl
from jax.experimental.pallas import tpu as pltpu

F32 = jnp.float32
BF16 = jnp.bfloat16
MESH = pl.DeviceIdType.MESH

HEAD_DIM = 128
HEADS_PER_GROUP = 4
DILATED_PATTERNS = ((128, 1), (512, 4), (2048, 16))
N_GROUPS = 3
N_HEADS = N_GROUPS * HEADS_PER_GROUP
SPAN = 128
GROUP_WIDTH = HEADS_PER_GROUP * HEAD_DIM
CONV_WIDTH = 31
FFN_CONV_WIDTH = 3
N_BUCKETS = 32
MAX_DISTANCE = 2048
RMS_EPS = 1e-6
LN_EPS = 1e-5
NEG_INF = -1e30
ADAM_LR = 0.001
ADAM_B1 = 0.9
ADAM_B2 = 0.999
ADAM_EPS = 1e-08
ADAM_WD = 0.01
ADAM_STEP = 10

LANES = 128
SUBLANES = 8
ROW_TILE = 512
GATE_ROWS, GATE_COLS = 512, 512
TIME_BLOCK = 128
CONV_PAD = 32
FFN_PAD = 8
VMEM_LIMIT = 56 << 20


def _params(sem=None, vmem=None):
    kw = {}
    if sem is not None:
        kw["dimension_semantics"] = sem
    if vmem is not None:
        kw["vmem_limit_bytes"] = vmem
    return pltpu.CompilerParams(**kw)


def _pick(n, cands):
    for c in cands:
        if n % c == 0:
            return c
    return n


ELEMENTWISE_TILE_BYTES = 3 << 19


def _row_tile(rows, cols):
    for align in (16, SUBLANES):
        fits = [t for t in range(align, rows + 1, align) if rows % t == 0 and t * cols * 4 <= ELEMENTWISE_TILE_BYTES]
        if fits:
            return max(fits)
    return SUBLANES


N_CHIPS = 4
M_TILES = (1024, 1408, 512, 256, 128)
N_TILES = (1024, 512, 1408, 256, 128)
K_TILES = (2176, 2048, 1408, 1024, 512, 256, 128)


def _matmul(a, b, mode, name, out_shards=False, tm=None):
    assert a.dtype == BF16 and b.dtype == BF16, (name, a.dtype, b.dtype)
    b3 = b.ndim == 3
    tn = tk = None
    halves = None
    if mode == "nn":
        M, K = a.shape
        N = b.shape[-1] * (N_CHIPS if b3 else 1)
        tn = b.shape[-1] if b3 else None
    elif mode == "nt":
        if a.ndim == 3:
            halves = a.shape[2]
        M, K = a.shape[-2], a.shape[-1] * (a.shape[0] if a.ndim == 3 else 1)
        N = b.shape[-2]
        tk = b.shape[-1] if b3 else None
    else:
        if b3:
            halves = b.shape[2]
        K, M = a.shape
        N = b.shape[-1] * (b.shape[0] if b3 else 1)
        tn = N // N_CHIPS if out_shards else None
    tm = tm or _pick(M, M_TILES)
    tn = tn or _pick(N, N_TILES)
    tk = tk or _pick(K, K_TILES)
    nk = K // tk
    dn = {"nn": (((1,), (0,)), ((), ())), "nt": (((1,), (1,)), ((), ())), "tn": (((0,), (0,)), ((), ()))}[mode]

    def body(a_ref, b_ref, o_ref):
        if nk == 1:
            o_ref[...] = lax.dot_general(a_ref[...], b_ref[...], dn, preferred_element_type=F32)
        else:
            @pl.when(pl.program_id(2) == 0)
            def _():
                o_ref[...] = jnp.zeros_like(o_ref)

            o_ref[...] += lax.dot_general(a_ref[...], b_ref[...], dn, preferred_element_type=F32)

    if mode == "tn":
        a_spec = pl.BlockSpec((tk, tm), lambda i, j, k: (k, i))
    elif halves:
        per = halves // tk
        a_spec = pl.BlockSpec((None, tm, tk), lambda i, j, k: (k // per, i, k % per))
    else:
        a_spec = pl.BlockSpec((tm, tk), lambda i, j, k: (i, k))
    if mode == "nn":
        b_spec = pl.BlockSpec((None, tk, tn), lambda i, j, k: (j, k, 0)) if b3 else pl.BlockSpec((tk, tn), lambda i, j, k: (k, j))
    elif mode == "nt":
        b_spec = pl.BlockSpec((None, tn, tk), lambda i, j, k: (k, j, 0)) if b3 else pl.BlockSpec((tn, tk), lambda i, j, k: (j, k))
    elif halves:
        per = halves // tn
        b_spec = pl.BlockSpec((None, tk, tn), lambda i, j, k: (j // per, k, j % per))
    else:
        b_spec = pl.BlockSpec((tk, tn), lambda i, j, k: (k, j))
    if out_shards:
        out_spec = pl.BlockSpec((None, tm, tn), lambda i, j, k: (j, i, 0))
        out_shape = jax.ShapeDtypeStruct((N_CHIPS, M, tn), F32)
    else:
        out_spec = pl.BlockSpec((tm, tn), lambda i, j, k: (i, j))
        out_shape = jax.ShapeDtypeStruct((M, N), F32)
    return pl.pallas_call(
        body, name=name, grid=(M // tm, N // tn, nk),
        in_specs=[a_spec, b_spec], out_specs=out_spec, out_shape=out_shape,
        compiler_params=_params(("parallel", "parallel", "arbitrary"), VMEM_LIMIT),
    )(a, b)


def _rms(x, g):
    r = lax.rsqrt(jnp.mean(x * x, axis=-1, keepdims=True) + RMS_EPS)
    return x * r * g


def _rms_bwd(x, g, dy):
    r = lax.rsqrt(jnp.mean(x * x, axis=-1, keepdims=True) + RMS_EPS)
    n = x * r
    dn = dy * g
    dx = r * (dn - n * jnp.mean(dn * n, axis=-1, keepdims=True))
    return dx, jnp.sum(dy * n, axis=0, keepdims=True)


def _sigmoid(x):
    return 1.0 / (1.0 + jnp.exp(-x))


_GELU_C = math.sqrt(2.0 / math.pi)


def _gelu(x):
    return 0.5 * x * (1.0 + jnp.tanh(_GELU_C * (x + 0.044715 * x * x * x)))


def _gelu_and_grad(x):
    x2 = x * x
    t = jnp.tanh(_GELU_C * x * (1.0 + 0.044715 * x2))
    half = 0.5 * (1.0 + t)
    return x * half, half + (0.5 * _GELU_C) * x * (1.0 - t * t) * (1.0 + (3.0 * 0.044715) * x2)


def _row_spec(width, col_block=0):
    return pl.BlockSpec((ROW_TILE, width), lambda i: (i, col_block))


def _vec_spec(width, col_block=0):
    return pl.BlockSpec((1, width), lambda i: (0, col_block))


def _accumulate(ref, part):
    @pl.when(pl.program_id(0) == 0)
    def _():
        ref[...] = part

    @pl.when(pl.program_id(0) > 0)
    def _():
        ref[...] += part


def _rms_fwd_call(x, g):
    S, D = x.shape

    def body(x_ref, g_ref, h_ref):
        h_ref[...] = _rms(x_ref[...], g_ref[...]).astype(BF16)

    return pl.pallas_call(
        body, name="rms_mix_pre", grid=(S // ROW_TILE,),
        in_specs=[_row_spec(D), _vec_spec(D)], out_specs=_row_spec(D),
        out_shape=jax.ShapeDtypeStruct((S, D), BF16),
        compiler_params=_params(("parallel",)),
    )(x, g)


def _ln_silu_call(c1, g, b):
    S, C = c1.shape

    def body(c_ref, g_ref, b_ref, o_ref):
        xv = c_ref[...]
        mu = jnp.mean(xv, axis=-1, keepdims=True)
        xc = xv - mu
        var = jnp.mean(xc * xc, axis=-1, keepdims=True)
        z = xc * lax.rsqrt(var + LN_EPS) * g_ref[...] + b_ref[...]
        o_ref[...] = (z * _sigmoid(z)).astype(BF16)

    return pl.pallas_call(
        body, name="conv_ln_silu", grid=(S // ROW_TILE,),
        in_specs=[_row_spec(C), _vec_spec(C), _vec_spec(C)], out_specs=_row_spec(C),
        out_shape=jax.ShapeDtypeStruct((S, C), BF16),
        compiler_params=_params(("parallel",)),
    )(c1, g, b)


def _ln_silu_bwd_call(c1, g, b, dc):
    S, C = c1.shape

    def body(c_ref, g_ref, b_ref, dc_ref, dx_ref, dg_ref, db_ref):
        xv = c_ref[...]
        mu = jnp.mean(xv, axis=-1, keepdims=True)
        xc = xv - mu
        rs = lax.rsqrt(jnp.mean(xc * xc, axis=-1, keepdims=True) + LN_EPS)
        xh = xc * rs
        z = xh * g_ref[...] + b_ref[...]
        sg = _sigmoid(z)
        dz = dc_ref[...] * (sg * (1.0 + z * (1.0 - sg)))
        dxh = dz * g_ref[...]
        dx_ref[...] = rs * (dxh - jnp.mean(dxh, axis=-1, keepdims=True) - xh * jnp.mean(dxh * xh, axis=-1, keepdims=True))
        _accumulate(dg_ref, jnp.sum(dz * xh, axis=0, keepdims=True))
        _accumulate(db_ref, jnp.sum(dz, axis=0, keepdims=True))

    return pl.pallas_call(
        body, name="conv_ln_silu_bwd", grid=(S // ROW_TILE,),
        in_specs=[_row_spec(C), _vec_spec(C), _vec_spec(C), _row_spec(C)],
        out_specs=[_row_spec(C), _vec_spec(C), _vec_spec(C)],
        out_shape=[jax.ShapeDtypeStruct((S, C), F32), jax.ShapeDtypeStruct((1, C), F32), jax.ShapeDtypeStruct((1, C), F32)],
        compiler_params=_params(("arbitrary",)),
    )(c1, g, b, dc)


def _mix_call(proj, gate_col0, b_gate, y_a, y_c):
    S, D = y_a.shape
    w = GATE_COLS
    nc = D // w
    ga0, gc0 = gate_col0 // w, (gate_col0 + D) // w

    def body(ga_ref, gc_ref, ba_ref, bc_ref, ya_ref, yc_ref, o_ref):
        o_ref[...] = (_sigmoid(ga_ref[...] + ba_ref[...]) * ya_ref[...]
                      + _sigmoid(gc_ref[...] + bc_ref[...]) * yc_ref[...]).astype(BF16)

    tile = lambda off: pl.BlockSpec((GATE_ROWS, w), lambda i, j: (i, off + j))
    vec = lambda off: pl.BlockSpec((1, w), lambda i, j: (0, off + j))
    return pl.pallas_call(
        body, name="gate_mix", grid=(S // GATE_ROWS, nc),
        in_specs=[tile(ga0), tile(gc0), vec(0), vec(nc), tile(0), tile(0)],
        out_specs=tile(0), out_shape=jax.ShapeDtypeStruct((S, D), BF16),
        compiler_params=_params(("parallel", "parallel")),
    )(proj, proj, b_gate, b_gate, y_a, y_c)


def _mix_bwd_call(dmixed, proj, gate_col0, b_gate, y_a, y_c):
    S, D = y_a.shape
    w = GATE_COLS
    nc = D // w
    ga0, gc0 = gate_col0 // w, (gate_col0 + D) // w

    def body(dm_ref, ga_ref, gc_ref, ba_ref, bc_ref, ya_ref, yc_ref, dya_ref, dyc_ref, dga_ref, dgc_ref, dba_ref, dbc_ref):
        dm = dm_ref[...]
        sa = _sigmoid(ga_ref[...] + ba_ref[...])
        sc = _sigmoid(gc_ref[...] + bc_ref[...])
        dya_ref[...] = (dm * sa).astype(BF16)
        dyc_ref[...] = (dm * sc).astype(BF16)
        dga = dm * ya_ref[...] * sa * (1.0 - sa)
        dgc = dm * yc_ref[...] * sc * (1.0 - sc)
        dga_ref[...] = dga.astype(BF16)
        dgc_ref[...] = dgc.astype(BF16)
        pa = jnp.sum(dga, axis=0, keepdims=True)
        pc = jnp.sum(dgc, axis=0, keepdims=True)

        @pl.when(pl.program_id(1) == 0)
        def _():
            dba_ref[...] = pa
            dbc_ref[...] = pc

        @pl.when(pl.program_id(1) > 0)
        def _():
            dba_ref[...] += pa
            dbc_ref[...] += pc

    tile = lambda off: pl.BlockSpec((GATE_ROWS, w), lambda j, i: (i, off + j))
    vec = lambda off: pl.BlockSpec((1, w), lambda j, i: (0, off + j))
    return pl.pallas_call(
        body, name="gate_mix_bwd", grid=(nc, S // GATE_ROWS),
        in_specs=[tile(0), tile(ga0), tile(gc0), vec(0), vec(nc), tile(0), tile(0)],
        out_specs=[tile(0), tile(0), tile(0), tile(0), vec(0), vec(0)],
        out_shape=[jax.ShapeDtypeStruct((S, D), BF16)] * 4 + [
                   jax.ShapeDtypeStruct((1, D), F32), jax.ShapeDtypeStruct((1, D), F32)],
        compiler_params=_params(("parallel", "arbitrary")),
    )(dmixed, proj, proj, b_gate, b_gate, y_a, y_c)


def _res1_call(x, out, g_post, g_pre):
    S, D = x.shape

    def body(x_ref, o_ref, gp_ref, gq_ref, x1_ref, h2_ref):
        x1 = x_ref[...] + _rms(o_ref[...], gp_ref[...])
        x1_ref[...] = x1
        h2_ref[...] = _rms(x1, gq_ref[...]).astype(BF16)

    return pl.pallas_call(
        body, name="residual_mix", grid=(S // ROW_TILE,),
        in_specs=[_row_spec(D), _row_spec(D), _vec_spec(D), _vec_spec(D)],
        out_specs=[_row_spec(D), _row_spec(D)],
        out_shape=[jax.ShapeDtypeStruct((S, D), F32), jax.ShapeDtypeStruct((S, D), BF16)],
        compiler_params=_params(("parallel",)),
    )(x, out, g_post, g_pre)


def _loss_call(y, x1, g_post, target):
    S, D = y.shape

    def body(y_ref, x1_ref, g_ref, t_ref, loss_ref, dx_ref, dy_ref, dg_ref):
        yv, gv = y_ref[...], g_ref[...]
        err = x1_ref[...] + _rms(yv, gv) - t_ref[...]
        dx2 = err * (1.0 / D)
        dx_ref[...] = dx2
        dy, dg = _rms_bwd(yv, gv, dx2)
        dy_ref[...] = dy.astype(BF16)
        _accumulate(dg_ref, dg)
        part = 0.5 * jnp.sum(jnp.mean(err * err, axis=-1, keepdims=True), axis=0, keepdims=True)
        _accumulate(loss_ref, jnp.broadcast_to(part, (SUBLANES, LANES)))

    return pl.pallas_call(
        body, name="residual_ffn_loss", grid=(S // ROW_TILE,),
        in_specs=[_row_spec(D), _row_spec(D), _vec_spec(D), _row_spec(D)],
        out_specs=[pl.BlockSpec((SUBLANES, LANES), lambda i: (0, 0)), _row_spec(D), _row_spec(D), _vec_spec(D)],
        out_shape=[jax.ShapeDtypeStruct((SUBLANES, LANES), F32), jax.ShapeDtypeStruct((S, D), F32),
                   jax.ShapeDtypeStruct((S, D), BF16), jax.ShapeDtypeStruct((1, D), F32)],
        compiler_params=_params(("arbitrary",)),
    )(y, x1, g_post, target)


def _mid_bwd_call(x1, g_pre, dh2, dx2, out, g_post):
    S, D = x1.shape

    def body(x1_ref, gq_ref, dh_ref, dx2_ref, o_ref, gp_ref, dx1_ref, do_ref, dgq_ref, dgp_ref):
        d, dgq = _rms_bwd(x1_ref[...], gq_ref[...], dh_ref[...])
        dx1 = dx2_ref[...] + d
        dx1_ref[...] = dx1
        do, dgp = _rms_bwd(o_ref[...], gp_ref[...], dx1)
        do_ref[...] = do.astype(BF16)
        _accumulate(dgq_ref, dgq)
        _accumulate(dgp_ref, dgp)

    return pl.pallas_call(
        body, name="residual_mix_bwd", grid=(S // ROW_TILE,),
        in_specs=[_row_spec(D), _vec_spec(D), _row_spec(D), _row_spec(D), _row_spec(D), _vec_spec(D)],
        out_specs=[_row_spec(D), _row_spec(D), _vec_spec(D), _vec_spec(D)],
        out_shape=[jax.ShapeDtypeStruct((S, D), F32), jax.ShapeDtypeStruct((S, D), BF16)] + [jax.ShapeDtypeStruct((1, D), F32)] * 2,
        compiler_params=_params(("arbitrary",)),
    )(x1, g_pre, dh2, dx2, out, g_post)


def _in_bwd_call(x, g, dh1, dx1):
    S, D = x.shape

    def body(x_ref, g_ref, dh_ref, dx1_ref, gx_ref, dg_ref):
        d, dg = _rms_bwd(x_ref[...], g_ref[...], dh_ref[...])
        gx_ref[...] = dx1_ref[...] + d
        _accumulate(dg_ref, dg)

    return pl.pallas_call(
        body, name="rms_mix_pre_bwd", grid=(S // ROW_TILE,),
        in_specs=[_row_spec(D), _vec_spec(D), _row_spec(D), _row_spec(D)],
        out_specs=[_row_spec(D), _vec_spec(D)],
        out_shape=[jax.ShapeDtypeStruct((S, D), F32), jax.ShapeDtypeStruct((1, D), F32)],
        compiler_params=_params(("arbitrary",)),
    )(x, g, dh1, dx1)


def _bucket_table(dilation):
    qi = np.arange(SPAN)[:, None]
    ki = np.arange(2 * SPAN)[None, :]
    dist = np.maximum(qi + SPAN - ki, 0) * dilation
    max_exact = N_BUCKETS // 2
    d = np.maximum(dist, 1).astype(np.float64)
    large = max_exact + (np.log(d / max_exact) / math.log(MAX_DISTANCE / max_exact) * (N_BUCKETS - max_exact)).astype(np.int32)
    large = np.minimum(large, N_BUCKETS - 1)
    return np.where(dist < max_exact, dist, large).astype(np.int32)


def _bucket_tables():
    return jnp.asarray(np.stack([_bucket_table(r) for _, r in DILATED_PATTERNS]))


def _bias_table_call(rel_bias, buckets):
    def body(rb_ref, bk_ref, o_ref):
        for h in range(N_HEADS):
            bk = bk_ref[h // HEADS_PER_GROUP]

            def step(b, acc):
                return jnp.where(bk == b, rb_ref[b, h], acc)

            o_ref[h] = lax.fori_loop(0, N_BUCKETS, step, jnp.zeros((SPAN, 2 * SPAN), F32))

    return pl.pallas_call(
        body, name="rel_bias_table",
        in_specs=[pl.BlockSpec(memory_space=pltpu.SMEM), pl.BlockSpec(memory_space=pltpu.VMEM)],
        out_specs=pl.BlockSpec(memory_space=pltpu.VMEM),
        out_shape=jax.ShapeDtypeStruct((N_HEADS, SPAN, 2 * SPAN), F32),
    )(rel_bias, buckets)


def _bias_grad_call(dbias, buckets):
    def body(db_ref, bk_ref, o_ref, rows_ref):
        for h in range(N_HEADS):
            bk = bk_ref[h // HEADS_PER_GROUP]
            dv = db_ref[h]

            def step(b, carry):
                rows_ref[h, b] = jnp.sum(jnp.where(bk == b, dv, 0.0), axis=0, keepdims=True)
                return carry

            lax.fori_loop(0, N_BUCKETS, step, 0)
        o_ref[...] = jnp.sum(rows_ref[...], axis=-1, keepdims=True)

    out = pl.pallas_call(
        body, name="rel_bias_grad",
        in_specs=[pl.BlockSpec(memory_space=pltpu.VMEM), pl.BlockSpec(memory_space=pltpu.VMEM)],
        out_specs=pl.BlockSpec(memory_space=pltpu.VMEM),
        out_shape=jax.ShapeDtypeStruct((N_HEADS, N_BUCKETS, 1, 1), F32),
        scratch_shapes=[pltpu.VMEM((N_HEADS, N_BUCKETS, 1, 2 * SPAN), F32)],
    )(dbias, buckets)
    return out.reshape(N_HEADS, N_BUCKETS).T


def _dot_nt(a, b):
    return lax.dot_general(a, b, (((1,), (1,)), ((), ())), preferred_element_type=F32)


def _dot_nn(a, b):
    return lax.dot_general(a, b, (((1,), (0,)), ((), ())), preferred_element_type=F32)


def _dot_tn(a, b):
    return lax.dot_general(a, b, (((0,), (0,)), ((), ())), preferred_element_type=F32)


def _band_masks(n, nb):
    qi = lax.broadcasted_iota(jnp.int32, (SPAN, SPAN), 0)
    ki = lax.broadcasted_iota(jnp.int32, (SPAN, SPAN), 1)
    prev_ok = jnp.logical_and(ki >= qi, n > 0)
    cur_ok = ki <= qi
    next_ok = jnp.logical_and(ki >= qi, n < nb - 1)
    return prev_ok, cur_ok, next_ok


def _wide_band_mask(n):
    qi = lax.broadcasted_iota(jnp.int32, (SPAN, 2 * SPAN), 0)
    ki = lax.broadcasted_iota(jnp.int32, (SPAN, 2 * SPAN), 1)
    prev_ok = jnp.logical_and(jnp.logical_and(ki < SPAN, ki >= qi), n > 0)
    cur_ok = jnp.logical_and(ki >= SPAN, ki - SPAN <= qi)
    return jnp.logical_or(prev_ok, cur_ok)


def _attn_plan(S, group):
    r = DILATED_PATTERNS[group][1]
    hp, per = (HEADS_PER_GROUP, 1) if r == 1 else (2, 4)
    return r, S // (r * SPAN), hp, per


def _residue_rows(rho, r):
    return slice(None) if r == 1 else pl.ds(rho, SPAN, stride=r)


def _for_residues(r, per, fn):
    if r == per:
        for u in range(per):
            fn(u)
        return

    def step(i, carry):
        for u in range(per):
            fn(i * per + u)
        return carry

    lax.fori_loop(0, r // per, step, 0)


def _attn_fwd_call(proj, bias, group):
    S = proj.shape[0]
    r, nb, hp, per = _attn_plan(S, group)
    scale = HEAD_DIM ** -0.5
    kinds = ("q", "kp", "kc", "vp", "vc") if nb > 1 else ("q", "kc", "vc")

    def body(*refs):
        ins = {kind: refs[i * hp:(i + 1) * hp] for i, kind in enumerate(kinds)}
        b_ref, o_ref, lse_ref = refs[len(kinds) * hp:]
        n = pl.program_id(1)
        prev_ok, cur_ok, _ = _band_masks(n, nb)

        band_ok = _wide_band_mask(n) if nb > 1 else cur_ok

        def residue(rho):
            rows = _residue_rows(rho, r)
            for j in range(hp):
                get = lambda kind: ins[kind][j][rows, :].astype(BF16)
                q = get("q")
                if nb > 1:
                    keys, vals, bias_j = jnp.concatenate([get("kp"), get("kc")], axis=0), jnp.concatenate([get("vp"), get("vc")], axis=0), b_ref[j]
                else:
                    keys, vals, bias_j = get("kc"), get("vc"), b_ref[j, :, SPAN:]
                s = jnp.where(band_ok, _dot_nt(q, keys) * scale + bias_j, NEG_INF)
                m = jnp.max(s, axis=-1, keepdims=True)
                p = jnp.exp(s - m)
                den = jnp.sum(p, axis=-1, keepdims=True)
                o_ref[j, rows, :] = _dot_nn(p.astype(BF16), vals) / den
                lse_ref[j, rows, :] = jnp.broadcast_to(m + jnp.log(den), (SPAN, HEAD_DIM))

        _for_residues(r, per, residue)

    in_specs = [_head_spec(r, nb, hp, kind, group, jj) for kind in kinds for jj in range(hp)]
    in_specs.append(pl.BlockSpec((hp, SPAN, 2 * SPAN), lambda j, n: (group * (HEADS_PER_GROUP // hp) + j, 0, 0)))
    out = pl.BlockSpec((hp, r * SPAN, HEAD_DIM), lambda j, n: (j, n, 0))
    return pl.pallas_call(
        body, name=f"attn_fwd_g{group}", grid=(HEADS_PER_GROUP // hp, nb),
        in_specs=in_specs, out_specs=[out] * 2,
        out_shape=[jax.ShapeDtypeStruct((HEADS_PER_GROUP, S, HEAD_DIM), F32)] * 2,
        compiler_params=_params(("parallel", "parallel"), VMEM_LIMIT),
    )(*([proj] * (len(in_specs) - 1)), bias)


_PROJ_PART = dict(q=0, qn=0, kp=1, kc=1, vp=2, vc=2)


def _head_spec(r, nb, hp, kind, group, jj):
    if kind in _PROJ_PART:
        base = (_PROJ_PART[kind] * N_GROUPS + group) * HEADS_PER_GROUP
    else:
        base = 0
    if kind.endswith("p"):
        row = lambda n: jnp.maximum(n - 1, 0)
    elif kind.endswith("n"):
        row = lambda n: jnp.minimum(n + 1, nb - 1)
    else:
        row = lambda n: n
    return pl.BlockSpec((r * SPAN, HEAD_DIM), lambda j, n: (row(n), base + j * hp + jj))


def _attn_merge_call(parts):
    S = parts[0].shape[1]

    def body(o1, s1, o2, s2, o3, s3, a_ref, ab_ref, lse_ref):
        for j in range(HEADS_PER_GROUP):
            sl = slice(j * HEAD_DIM, (j + 1) * HEAD_DIM)
            mx = jnp.maximum(jnp.maximum(s1[j], s2[j]), s3[j])
            w1 = jnp.exp(s1[j] - mx)
            w2 = jnp.exp(s2[j] - mx)
            w3 = jnp.exp(s3[j] - mx)
            den = w1 + w2 + w3
            a = (w1 * o1[j] + w2 * o2[j] + w3 * o3[j]) / den
            a_ref[:, sl] = a
            ab_ref[:, sl] = a.astype(BF16)
            lse_ref[:, sl] = mx + jnp.log(den)

    heads = pl.BlockSpec((HEADS_PER_GROUP, ROW_TILE, HEAD_DIM), lambda i: (0, i, 0))
    return pl.pallas_call(
        body, name="attn_merge", grid=(S // ROW_TILE,),
        in_specs=[heads] * 6, out_specs=[_row_spec(GROUP_WIDTH)] * 3,
        out_shape=[jax.ShapeDtypeStruct((S, GROUP_WIDTH), F32), jax.ShapeDtypeStruct((S, GROUP_WIDTH), BF16),
                   jax.ShapeDtypeStruct((S, GROUP_WIDTH), F32)],
        compiler_params=_params(("parallel",)),
    )(*parts)


def _attn_delta_call(a, da):
    S = a.shape[0]

    def body(a_ref, da_ref, d_ref):
        for j in range(HEADS_PER_GROUP):
            sl = slice(j * HEAD_DIM, (j + 1) * HEAD_DIM)
            d = jnp.sum(a_ref[:, sl] * da_ref[:, sl], axis=-1, keepdims=True)
            d_ref[:, sl] = jnp.broadcast_to(d, (ROW_TILE, HEAD_DIM))

    return pl.pallas_call(
        body, name="attn_delta", grid=(S // ROW_TILE,),
        in_specs=[_row_spec(GROUP_WIDTH)] * 2, out_specs=_row_spec(GROUP_WIDTH),
        out_shape=jax.ShapeDtypeStruct((S, GROUP_WIDTH), F32),
        compiler_params=_params(("parallel",)),
    )(a, da)


def _attn_bwd_call(proj, bias, da, lse, delta, group):
    S = proj.shape[0]
    r, nb, hp, per = _attn_plan(S, group)
    scale = HEAD_DIM ** -0.5
    kinds = ("q", "qn", "kp", "kc", "vp", "vc", "da", "dan", "lse", "lsen", "dl", "dln") if nb > 1 else ("q", "kc", "vc", "da", "lse", "dl")
    source = dict(da=da, dan=da, lse=lse, lsen=lse, dl=delta, dln=delta)

    def body(*refs):
        ins = {kind: refs[i * hp:(i + 1) * hp] for i, kind in enumerate(kinds)}
        b_ref, dq_ref, dk_ref, dv_ref, db_ref = refs[len(kinds) * hp:]
        n = pl.program_id(1)
        prev_ok, cur_ok, next_ok = _band_masks(n, nb)

        @pl.when(n == 0)
        def _():
            db_ref[...] = jnp.zeros_like(db_ref)

        band_ok = _wide_band_mask(n) if nb > 1 else cur_ok

        def residue(rho):
            rows = _residue_rows(rho, r)
            for j in range(hp):
                get = lambda kind: ins[kind][j][rows, :]
                q = get("q").astype(BF16)
                kc = get("kc").astype(BF16)
                vc = get("vc").astype(BF16)
                dav = get("da").astype(BF16)
                lse_q, dl_q = get("lse"), get("dl")
                if nb == 1:
                    pc = jnp.exp(jnp.where(cur_ok, _dot_nt(q, kc) * scale + b_ref[j, :, SPAN:], NEG_INF) - lse_q)
                    dsc = pc * (_dot_nt(dav, vc) - dl_q)
                    dsc_b = dsc.astype(BF16)
                    dq = _dot_nn(dsc_b, kc)
                    dk = _dot_tn(dsc_b, q)
                    dv = _dot_tn(pc.astype(BF16), dav)
                    db_ref[j, :, SPAN:] += dsc
                else:
                    qn = get("qn").astype(BF16)
                    dan = get("dan").astype(BF16)
                    keys = jnp.concatenate([get("kp").astype(BF16), kc], axis=0)
                    vals = jnp.concatenate([get("vp").astype(BF16), vc], axis=0)
                    wide = lambda t: jnp.concatenate([t, t], axis=1)
                    p = jnp.exp(jnp.where(band_ok, _dot_nt(q, keys) * scale + b_ref[j], NEG_INF) - wide(lse_q))
                    ds = p * (_dot_nt(dav, vals) - wide(dl_q))
                    dq = _dot_nn(ds.astype(BF16), keys)
                    db_ref[j] += ds
                    pn = jnp.exp(jnp.where(next_ok, _dot_nt(qn, kc) * scale + b_ref[j, :, :SPAN], NEG_INF) - get("lsen"))
                    dsn = pn * (_dot_nt(dan, vc) - get("dln"))
                    both = lambda cur_part, next_part: jnp.concatenate([cur_part.astype(BF16), next_part.astype(BF16)], axis=0)
                    dk = _dot_tn(both(ds[:, SPAN:], dsn), jnp.concatenate([q, qn], axis=0))
                    dv = _dot_tn(both(p[:, SPAN:], pn), jnp.concatenate([dav, dan], axis=0))
                dq_ref[j, rows, :] = dq * scale
                dk_ref[j, rows, :] = dk * scale
                dv_ref[j, rows, :] = dv

        _for_residues(r, per, residue)

    per_group = HEADS_PER_GROUP // hp
    band = (hp, SPAN, 2 * SPAN)
    in_specs = [_head_spec(r, nb, hp, kind, group, jj) for kind in kinds for jj in range(hp)]
    in_specs.append(pl.BlockSpec(band, lambda j, n: (group * per_group + j, 0, 0)))
    operands = [source.get(kind, proj) for kind in kinds for _ in range(hp)] + [bias]
    out = pl.BlockSpec((hp, r * SPAN, HEAD_DIM), lambda j, n: (j, n, 0))
    return pl.pallas_call(
        body, name=f"attn_bwd_g{group}", grid=(per_group, nb),
        in_specs=in_specs,
        out_specs=[out] * 3 + [pl.BlockSpec(band, lambda j, n: (j, 0, 0))],
        out_shape=[jax.ShapeDtypeStruct((HEADS_PER_GROUP, S, HEAD_DIM), F32)] * 3
        + [jax.ShapeDtypeStruct((HEADS_PER_GROUP, SPAN, 2 * SPAN), F32)],
        compiler_params=_params(("parallel", "arbitrary"), VMEM_LIMIT),
    )(*operands)


def _dproj_call(dqkv, tails):
    S = tails[0].shape[0]
    width = len(dqkv) * GROUP_WIDTH + sum(t.shape[1] for t in tails)

    def body(*refs):
        o_ref = refs[-1]
        col = 0
        for ref in refs[:len(dqkv)]:
            for j in range(HEADS_PER_GROUP):
                o_ref[:, col:col + HEAD_DIM] = ref[j].astype(BF16)
                col += HEAD_DIM
        for ref in refs[len(dqkv):-1]:
            o_ref[:, col:col + ref.shape[1]] = ref[...]
            col += ref.shape[1]

    heads = pl.BlockSpec((HEADS_PER_GROUP, ROW_TILE, HEAD_DIM), lambda i: (0, i, 0))
    return pl.pallas_call(
        body, name="dproj_assemble", grid=(S // ROW_TILE,),
        in_specs=[heads] * len(dqkv) + [_row_spec(t.shape[1]) for t in tails],
        out_specs=_row_spec(width), out_shape=jax.ShapeDtypeStruct((S, width), BF16),
        compiler_params=_params(("parallel",)),
    )(*dqkv, *tails)


def _tap_rows(xpad_ref, t0, k, width, pad):
    return xpad_ref[pl.ds(t0 + (pad - (width - 1 - k)), TIME_BLOCK), :]


def _conv_block(xpad_ref, t0, w_ref, width, pad):
    acc = None
    for k in range(width):
        term = w_ref[k:k + 1, :] * _tap_rows(xpad_ref, t0, k, width, pad)
        acc = term if acc is None else acc + term
    return acc


def _conv_transpose_block(dpad_ref, t0, w_ref, width):
    acc = None
    for k in range(width):
        term = w_ref[k:k + 1, :] * dpad_ref[pl.ds(t0 + (width - 1 - k), TIME_BLOCK), :]
        acc = term if acc is None else acc + term
    return acc


def _conv_weight_grad(xpad_ref, t0, dy, dw_ref, width, pad):
    for k in range(width):
        dw_ref[k:k + 1, :] += jnp.sum(dy * _tap_rows(xpad_ref, t0, k, width, pad), axis=0, keepdims=True)


def _time_loop(S, step):
    def it(tb, carry):
        step(pl.multiple_of(tb * TIME_BLOCK, TIME_BLOCK))
        return carry

    lax.fori_loop(0, S // TIME_BLOCK, it, 0)


def _conv_fwd_call(proj, col0, w, b):
    S = proj.shape[0]
    C = w.shape[1]
    nt = C // LANES
    v0, g0 = col0 // LANES, (col0 + C) // LANES

    def body(val_ref, gate_ref, w_ref, b_ref, o_ref, pad_ref):
        pad_ref[0:CONV_PAD, :] = jnp.zeros((CONV_PAD, LANES), F32)
        pad_ref[CONV_PAD:, :] = val_ref[...] * _sigmoid(gate_ref[...])

        def step(t0):
            o_ref[pl.ds(t0, TIME_BLOCK), :] = _conv_block(pad_ref, t0, w_ref, CONV_WIDTH, CONV_PAD) + b_ref[...]

        _time_loop(S, step)

    seq = lambda off: pl.BlockSpec((S, LANES), lambda i: (0, off + i))
    return pl.pallas_call(
        body, name="conv_module", grid=(nt,),
        in_specs=[seq(v0), seq(g0), pl.BlockSpec((CONV_WIDTH, LANES), lambda i: (0, i)), pl.BlockSpec((1, LANES), lambda i: (0, i))],
        out_specs=seq(0), out_shape=jax.ShapeDtypeStruct((S, C), F32),
        scratch_shapes=[pltpu.VMEM((S + CONV_PAD, LANES), F32)],
        compiler_params=_params(("parallel",)),
    )(proj, proj, w, b)


def _conv_bwd_call(proj, col0, w, dc1):
    S = proj.shape[0]
    C = w.shape[1]
    nt = C // LANES
    v0, g0 = col0 // LANES, (col0 + C) // LANES

    def body(val_ref, gate_ref, w_ref, dy_ref, dval_ref, dgate_ref, dw_ref, db_ref, xpad_ref, dpad_ref, dwacc_ref):
        xpad_ref[0:CONV_PAD, :] = jnp.zeros((CONV_PAD, LANES), F32)
        xpad_ref[CONV_PAD:, :] = val_ref[...] * _sigmoid(gate_ref[...])
        dpad_ref[0:S, :] = dy_ref[...]
        dpad_ref[S:, :] = jnp.zeros((CONV_PAD, LANES), F32)
        dwacc_ref[...] = jnp.zeros_like(dwacc_ref)

        def step(t0):
            rows = pl.ds(t0, TIME_BLOCK)
            _conv_weight_grad(xpad_ref, t0, dy_ref[rows, :], dwacc_ref, CONV_WIDTH, CONV_PAD)
            dc0 = _conv_transpose_block(dpad_ref, t0, w_ref, CONV_WIDTH)
            sg = _sigmoid(gate_ref[rows, :])
            dval_ref[rows, :] = (dc0 * sg).astype(BF16)
            dgate_ref[rows, :] = (dc0 * val_ref[rows, :] * sg * (1.0 - sg)).astype(BF16)

        _time_loop(S, step)
        dw_ref[...] = dwacc_ref[...]
        db_ref[...] = jnp.sum(dy_ref[...], axis=0, keepdims=True)

    seq = lambda off: pl.BlockSpec((S, LANES), lambda i: (0, off + i))
    return pl.pallas_call(
        body, name="conv_module_bwd", grid=(nt,),
        in_specs=[seq(v0), seq(g0), pl.BlockSpec((CONV_WIDTH, LANES), lambda i: (0, i)), seq(0)],
        out_specs=[seq(0), seq(0), pl.BlockSpec((CONV_PAD, LANES), lambda i: (0, i)), pl.BlockSpec((1, LANES), lambda i: (0, i))],
        out_shape=[jax.ShapeDtypeStruct((S, C), BF16), jax.ShapeDtypeStruct((S, C), BF16),
                   jax.ShapeDtypeStruct((CONV_PAD, C), F32), jax.ShapeDtypeStruct((1, C), F32)],
        scratch_shapes=[pltpu.VMEM((S + CONV_PAD, LANES), F32), pltpu.VMEM((S + CONV_PAD, LANES), F32),
                        pltpu.VMEM((CONV_PAD, LANES), F32)],
        compiler_params=_params(("parallel",)),
    )(proj, proj, w, dc1)


def _ffn_fwd_call(u, w, b):
    S, C2 = u.shape
    C = C2 // 2
    nt = C // LANES

    def body(ug_ref, uv_ref, wg_ref, wv_ref, bg_ref, bv_ref, f_ref, pg_ref, pv_ref):
        zeros = jnp.zeros((FFN_PAD, LANES), F32)
        pg_ref[0:FFN_PAD, :] = zeros
        pv_ref[0:FFN_PAD, :] = zeros
        pg_ref[FFN_PAD:, :] = ug_ref[...]
        pv_ref[FFN_PAD:, :] = uv_ref[...]

        def step(t0):
            cg = _conv_block(pg_ref, t0, wg_ref, FFN_CONV_WIDTH, FFN_PAD) + bg_ref[...]
            cv = _conv_block(pv_ref, t0, wv_ref, FFN_CONV_WIDTH, FFN_PAD) + bv_ref[...]
            f_ref[pl.ds(t0, TIME_BLOCK), :] = (_gelu(cg) * cv).astype(BF16)

        _time_loop(S, step)

    seq = lambda off: pl.BlockSpec((S, LANES), lambda i: (0, off + i))
    wsp = lambda off: pl.BlockSpec((FFN_CONV_WIDTH, LANES), lambda i: (0, off + i))
    bsp = lambda off: pl.BlockSpec((1, LANES), lambda i: (0, off + i))
    return pl.pallas_call(
        body, name="ffn_conv_geglu", grid=(nt,),
        in_specs=[seq(0), seq(nt), wsp(0), wsp(nt), bsp(0), bsp(nt)],
        out_specs=seq(0), out_shape=jax.ShapeDtypeStruct((S, C), BF16),
        scratch_shapes=[pltpu.VMEM((S + FFN_PAD, LANES), F32)] * 2,
        compiler_params=_params(("parallel",)),
    )(u, u, w, w, b, b)


def _ffn_bwd_call(u, w, b, df):
    S, C2 = u.shape
    C = C2 // 2
    nt = C // LANES

    def body(ug_ref, uv_ref, wg_ref, wv_ref, bg_ref, bv_ref, df_ref,
             du_ref, dwg_ref, dwv_ref, dbg_ref, dbv_ref,
             pg_ref, pv_ref, dg_ref, dv_ref, dwg_acc, dwv_acc, dbg_acc, dbv_acc):
        zeros = jnp.zeros((FFN_PAD, LANES), F32)
        pg_ref[0:FFN_PAD, :] = zeros
        pv_ref[0:FFN_PAD, :] = zeros
        pg_ref[FFN_PAD:, :] = ug_ref[...]
        pv_ref[FFN_PAD:, :] = uv_ref[...]
        dg_ref[S:, :] = zeros
        dv_ref[S:, :] = zeros
        dwg_acc[...] = jnp.zeros_like(dwg_acc)
        dwv_acc[...] = jnp.zeros_like(dwv_acc)
        dbg_acc[...] = jnp.zeros_like(dbg_acc)
        dbv_acc[...] = jnp.zeros_like(dbv_acc)

        def first(t0):
            rows = pl.ds(t0, TIME_BLOCK)
            cg = _conv_block(pg_ref, t0, wg_ref, FFN_CONV_WIDTH, FFN_PAD) + bg_ref[...]
            cv = _conv_block(pv_ref, t0, wv_ref, FFN_CONV_WIDTH, FFN_PAD) + bv_ref[...]
            dfb = df_ref[rows, :]
            gelu, gelu_grad = _gelu_and_grad(cg)
            dcg = dfb * cv * gelu_grad
            dcv = dfb * gelu
            dg_ref[rows, :] = dcg
            dv_ref[rows, :] = dcv
            _conv_weight_grad(pg_ref, t0, dcg, dwg_acc, FFN_CONV_WIDTH, FFN_PAD)
            _conv_weight_grad(pv_ref, t0, dcv, dwv_acc, FFN_CONV_WIDTH, FFN_PAD)
            dbg_acc[...] += jnp.sum(dcg, axis=0, keepdims=True)
            dbv_acc[...] += jnp.sum(dcv, axis=0, keepdims=True)

        def second(t0):
            rows = pl.ds(t0, TIME_BLOCK)
            du_ref[0, rows, :] = _conv_transpose_block(dg_ref, t0, wg_ref, FFN_CONV_WIDTH).astype(BF16)
            du_ref[1, rows, :] = _conv_transpose_block(dv_ref, t0, wv_ref, FFN_CONV_WIDTH).astype(BF16)

        _time_loop(S, first)
        _time_loop(S, second)
        dwg_ref[...] = dwg_acc[...]
        dwv_ref[...] = dwv_acc[...]
        dbg_ref[...] = dbg_acc[...]
        dbv_ref[...] = dbv_acc[...]

    seq = lambda off: pl.BlockSpec((S, LANES), lambda i: (0, off + i))
    wsp = lambda off: pl.BlockSpec((FFN_CONV_WIDTH, LANES), lambda i: (0, off + i))
    bsp = lambda off: pl.BlockSpec((1, LANES), lambda i: (0, off + i))
    return pl.pallas_call(
        body, name="ffn_conv_geglu_bwd", grid=(nt,),
        in_specs=[seq(0), seq(nt), wsp(0), wsp(nt), bsp(0), bsp(nt), seq(0)],
        out_specs=[pl.BlockSpec((2, S, LANES), lambda i: (0, 0, i)),
                   pl.BlockSpec((SUBLANES, LANES), lambda i: (0, i)), pl.BlockSpec((SUBLANES, LANES), lambda i: (0, i)),
                   bsp(0), bsp(0)],
        out_shape=[jax.ShapeDtypeStruct((2, S, C), BF16)] + [jax.ShapeDtypeStruct((SUBLANES, C), F32)] * 2
        + [jax.ShapeDtypeStruct((1, C), F32)] * 2,
        scratch_shapes=[pltpu.VMEM((S + FFN_PAD, LANES), F32)] * 4 + [pltpu.VMEM((SUBLANES, LANES), F32)] * 2
        + [pltpu.VMEM((1, LANES), F32)] * 2,
        compiler_params=_params(("parallel",)),
    )(u, u, w, w, b, b, df)


def _adamw(w_ref, g_ref, m_ref, v_ref, d_ref, mo_ref, vo_ref):
    gv = g_ref[...]
    mn = ADAM_B1 * m_ref[...] + (1.0 - ADAM_B1) * gv
    vn = ADAM_B2 * v_ref[...] + (1.0 - ADAM_B2) * (gv * gv)
    mo_ref[...] = mn
    vo_ref[...] = vn
    m_hat = mn * (1.0 / (1.0 - ADAM_B1 ** ADAM_STEP))
    v_hat = vn * (1.0 / (1.0 - ADAM_B2 ** ADAM_STEP))
    d_ref[...] = -ADAM_LR * (m_hat / (jnp.sqrt(v_hat) + ADAM_EPS) + ADAM_WD * w_ref[...])


def _adamw_call(w, g, m, v, name):
    R, C = w.shape
    tr = _row_tile(R, C)

    def body(w_ref, g_ref, m_ref, v_ref, go_ref, d_ref, mo_ref, vo_ref):
        go_ref[...] = g_ref[...]
        _adamw(w_ref, g_ref, m_ref, v_ref, d_ref, mo_ref, vo_ref)

    spec = pl.BlockSpec((tr, C), lambda i: (i, 0))
    return pl.pallas_call(
        body, name=name, grid=(R // tr,),
        in_specs=[spec] * 4, out_specs=[spec] * 4,
        out_shape=[jax.ShapeDtypeStruct((R, C), F32)] * 4,
        compiler_params=_params(("parallel",)),
    )(w, g, m, v)


def _adamw_small_call(ws, gs, ms, vs):
    n = len(ws)

    def body(*refs):
        w_refs, g_refs, m_refs, v_refs, d_refs, mo_refs, vo_refs = (refs[i * n:(i + 1) * n] for i in range(7))
        for i in range(n):
            _adamw(w_refs[i], g_refs[i], m_refs[i], v_refs[i], d_refs[i], mo_refs[i], vo_refs[i])

    whole = pl.BlockSpec(memory_space=pltpu.VMEM)
    outs = pl.pallas_call(
        body, name="adamw_small",
        in_specs=[whole] * (4 * n), out_specs=[whole] * (3 * n),
        out_shape=[jax.ShapeDtypeStruct(w.shape, F32) for w in ws] * 3,
    )(*ws, *gs, *ms, *vs)
    return outs[:n], outs[n:2 * n], outs[2 * n:]


def _position():
    return lax.axis_index("x"), lax.axis_index("y"), lax.axis_index("c")


def _chip_peers(x, y):
    return [(x, 1 - y), (1 - x, y), (1 - x, 1 - y)]


def _half_rows(ref, core, rows):
    h = rows // 2
    start = pl.multiple_of(core * h, 16)
    return ref.at[pl.ds(start, h), :] if len(ref.shape) == 2 else ref.at[:, pl.ds(start, h), :]


def _shard_half(ref, shard, core, rows):
    h = rows // 2
    return ref.at[shard, pl.ds(pl.multiple_of(core * h, 16), h), :]


ANY = pl.BlockSpec(memory_space=pl.ANY)


def _first_hop_copies(srcs, lands):
    x, y, c = _position()
    chip = 2 * x + y
    targets = [(px, py, c) for px, py in _chip_peers(x, y)] + [(x, y, 1 - c)]
    rows = srcs[0].shape[0]
    out = []
    for i, (s, l) in enumerate(zip(srcs, lands)):
        for k, dev in enumerate(targets):
            if i == 0 and k < 3:
                out.append((_half_rows(s, c, rows), _shard_half(l, chip, c, rows), dev, k))
            else:
                out.append((s, l.at[chip], dev, len(targets) * i + k))
    return out


def _second_hop_copies(srcs, lands):
    x, y, c = _position()
    rows = lands[0].shape[1]
    out = []
    for k, (px, py) in enumerate(_chip_peers(x, y)):
        half = _shard_half(lands[0], 2 * px + py, c, rows)
        out.append((half, half, (x, y, 1 - c), k))
    return out


HBM_SPEC = pl.BlockSpec(memory_space=pltpu.HBM)
SEM_SPEC = pl.BlockSpec(memory_space=pltpu.SEMAPHORE)
DATAFLOW = pltpu.SideEffectType.DATAFLOW_SIDE_EFFECTING


def _in_hbm(a):
    return pltpu.with_memory_space_constraint(a, pltpu.HBM)


def _split_start(name, groups, after, carry=None):
    spans, arrays = [], []
    for srcs, lands, _, _ in groups:
        spans.append((len(arrays), len(srcs), len(lands)))
        arrays += list(srcs) + list(lands)
    if carry is not None:
        arrays.append(carry)
    na, ng = len(arrays), len(groups)

    def body(*refs):
        sems, token = refs[na + 1:na + 1 + 2 * ng], refs[-1]
        for g, (_, _, _, copies) in enumerate(groups):
            off, ns, nl = spans[g]
            for src, dst, dev, idx in copies(refs[off:off + ns], refs[off + ns:off + ns + nl]):
                pltpu.make_async_remote_copy(src_ref=src, dst_ref=dst, send_sem=sems[2 * g].at[idx], recv_sem=sems[2 * g + 1].at[idx],
                                             device_id=dev, device_id_type=MESH).start()
        token[...] = jnp.zeros_like(token)

    outs = pl.pallas_call(
        body, name=name,
        in_specs=[HBM_SPEC] * na + [ANY],
        out_specs=[SEM_SPEC] * (2 * ng) + [HBM_SPEC] * na + [pl.BlockSpec(memory_space=pltpu.VMEM)],
        out_shape=[pltpu.SemaphoreType.DMA((n_sems,)) for _, _, n_sems, _ in groups for _ in range(2)]
        + [pltpu.HBM(a.shape, a.dtype) for a in arrays] + [jax.ShapeDtypeStruct((SUBLANES, LANES), F32)],
        input_output_aliases={i: 2 * ng + i for i in range(na)},
        compiler_params=pltpu.CompilerParams(has_side_effects=DATAFLOW),
    )(*[_in_hbm(a) for a in arrays], after)
    started = []
    for g, (off, ns, nl) in enumerate(spans):
        thru = outs[2 * ng + off:2 * ng + off + ns + nl]
        started.append(dict(send=outs[2 * g], recv=outs[2 * g + 1], srcs=list(thru[:ns]), lands=list(thru[ns:]),
                            tile=outs[-1], token=outs[-1][0, 0], carry=None if carry is None else outs[2 * ng + na - 1]))
    return started


def _split_wait(name, started, copies, after):
    n, m = len(started["srcs"]), len(started["lands"])

    def body(*refs):
        src_refs, land_refs = refs[:n], refs[n:n + m]
        send_sem, recv_sem = refs[n + m], refs[n + m + 1]
        for src, dst, dev, idx in copies(src_refs, land_refs):
            cp = pltpu.make_async_remote_copy(src_ref=src, dst_ref=dst, send_sem=send_sem.at[idx], recv_sem=recv_sem.at[idx],
                                              device_id=dev, device_id_type=MESH)
            cp.wait_send()
            cp.wait_recv()

    arrays = started["srcs"] + started["lands"]
    outs = pl.pallas_call(
        body, name=name,
        in_specs=[HBM_SPEC] * (n + m) + [SEM_SPEC, SEM_SPEC, ANY],
        out_specs=[HBM_SPEC] * (n + m),
        out_shape=[pltpu.HBM(a.shape, a.dtype) for a in arrays],
        input_output_aliases={i: i for i in range(n + m)},
        compiler_params=pltpu.CompilerParams(has_side_effects=DATAFLOW),
    )(*arrays, started["send"], started["recv"], after)
    return list(outs)


def _gather_copies(srcs, lands):
    x, y, c = _position()
    chip = 2 * x + y
    targets = [(px, py, c) for px, py in _chip_peers(x, y)] + [(x, y, 1 - c)]
    return [(s, l.at[chip], dev, len(targets) * i + k) for i, (s, l) in enumerate(zip(srcs, lands)) for k, dev in enumerate(targets)]


def _sibling_copies(srcs, lands):
    x, y, c = _position()
    return [(_half_rows(srcs[0], 1 - c, srcs[0].shape[1]), lands[0], (x, y, 1 - c), 0)]


def _exchange_copies(srcs, lands):
    x, y, c = _position()
    return [(srcs[0].at[2 * px + py], lands[0].at[k], (px, py, c), k) for k, (px, py) in enumerate(_chip_peers(x, y))]


def _pair_sum_call(grad, recv, core, name):
    _, h, B = recv.shape
    tr = _row_tile(h, B)

    def body(core_ref, g_ref, r_ref, o_ref, ob_ref):
        s = g_ref[...] + r_ref[...]
        o_ref[...] = s
        ob_ref[...] = s.astype(BF16)

    g_spec = pl.BlockSpec((None, tr, B), lambda q, i, core_ref: (q, core_ref[0] * (h // tr) + i, 0))
    spec = pl.BlockSpec((None, tr, B), lambda q, i, core_ref: (q, i, 0))
    return pl.pallas_call(
        body, name=name,
        grid_spec=pltpu.PrefetchScalarGridSpec(num_scalar_prefetch=1, grid=(N_CHIPS, h // tr), in_specs=[g_spec, spec],
                                               out_specs=[spec, spec]),
        out_shape=[jax.ShapeDtypeStruct(recv.shape, F32), jax.ShapeDtypeStruct(recv.shape, BF16)],
        compiler_params=_params(("parallel", "parallel")),
    )(core, grad, recv)


def _chip_sum_call(partial, recv, chip_core, name):
    _, h, B = recv.shape
    tr = _row_tile(h, B)

    def body(cc_ref, p_ref, r_ref, o_ref):
        o_ref[...] = ((p_ref[...] + r_ref[0].astype(F32)) + r_ref[1].astype(F32)) + r_ref[2].astype(F32)

    return pl.pallas_call(
        body, name=name,
        grid_spec=pltpu.PrefetchScalarGridSpec(
            num_scalar_prefetch=1, grid=(h // tr,),
            in_specs=[pl.BlockSpec((None, tr, B), lambda i, cc_ref: (cc_ref[0], i, 0)),
                      pl.BlockSpec((3, tr, B), lambda i, cc_ref: (0, i, 0))],
            out_specs=pl.BlockSpec((tr, B), lambda i, cc_ref: (cc_ref[1] * (h // tr) + i, 0))),
        out_shape=jax.ShapeDtypeStruct((2 * h, B), F32),
        compiler_params=_params(("parallel",)),
    )(chip_core, partial, recv)


def _sibling_assemble_call(shards, name="grad_sibling_assemble"):
    n = len(shards)

    def body(*refs):
        ins, outs = refs[:n], refs[n:2 * n]
        send_sems, recv_sems = refs[2 * n:]
        x, y, c = _position()
        copies = []
        for i in range(n):
            rows = shards[i].shape[0]
            cp = pltpu.make_async_remote_copy(src_ref=_half_rows(ins[i], c, rows), dst_ref=_half_rows(outs[i], c, rows),
                                              send_sem=send_sems.at[i], recv_sem=recv_sems.at[i],
                                              device_id=(x, y, 1 - c), device_id_type=MESH)
            cp.start()
            copies.append(cp)
        for cp in copies:
            cp.wait()

    return pl.pallas_call(
        body, name=name,
        in_specs=[ANY] * n, out_specs=[ANY] * n,
        out_shape=[jax.ShapeDtypeStruct(s.shape, F32) for s in shards],
        input_output_aliases={i: i for i in range(n)},
        scratch_shapes=[pltpu.SemaphoreType.DMA((n,)), pltpu.SemaphoreType.DMA((n,))],
    )(*shards)


N_DEVICES = 8


def _allsum_copies(srcs, lands):
    x, y, c = _position()
    me = 4 * x + 2 * y + c
    out = []
    for k in range(1, N_DEVICES):
        peer = (1 - x if k & 4 else x, 1 - y if k & 2 else y, 1 - c if k & 1 else c)
        out.append((srcs[0], lands[0].at[me], peer, k - 1))
    return out


def _ordered_sum_call(mine, landed, me):
    rows = mine.shape[0]

    def body(me_ref, x_ref, l_ref, o_ref):
        acc = jnp.where(me_ref[0] == 0, x_ref[...], l_ref[0])
        for d in range(1, N_DEVICES):
            acc = acc + jnp.where(me_ref[0] == d, x_ref[...], l_ref[d])
        o_ref[...] = acc

    return pl.pallas_call(
        body, name="small_grad_sum",
        in_specs=[pl.BlockSpec(memory_space=pltpu.SMEM), pl.BlockSpec(memory_space=pltpu.VMEM), pl.BlockSpec(memory_space=pltpu.VMEM)],
        out_specs=pl.BlockSpec(memory_space=pltpu.VMEM),
        out_shape=jax.ShapeDtypeStruct((rows, LANES), F32),
    )(me, mine, landed)


def _pack(arrays):
    flat = jnp.concatenate([a.reshape(-1).astype(F32) for a in arrays])
    rows = -(-flat.shape[0] // LANES)
    rows = -(-rows // SUBLANES) * SUBLANES
    flat = jnp.pad(flat, (0, rows * LANES - flat.shape[0]))
    return flat.reshape(rows, LANES)


def _unpack(packed, shapes):
    flat = packed.reshape(-1)
    out, off = [], 0
    for shp in shapes:
        size = int(np.prod(shp))
        out.append(flat[off:off + size].reshape(shp))
        off += size
    return out


def _local_step(xs, target, P, late_weights, on_grad):
    S, D = xs.shape
    qkv_width = 3 * N_HEADS * HEAD_DIM
    glu_col0, gate_col0 = qkv_width, qkv_width + 2 * D
    shard_major = lambda g: g.reshape(N_CHIPS, g.shape[0] // N_CHIPS, g.shape[1])

    h1 = _rms_fwd_call(xs, P["norm_mix_pre"])
    buckets = _bucket_tables()
    bias = _bias_table_call(P["rel_bias"] + 0.0 * h1[0, 0].astype(F32), buckets)
    P = dict(P, **late_weights("in", bias))
    proj = _matmul(h1, P["w_in"], "nn", "proj_in")
    parts = []
    for g in range(N_GROUPS):
        parts += _attn_fwd_call(proj, bias, g)
    a, a_bf, lse = _attn_merge_call(parts)
    P = dict(P, **late_weights("mix", a_bf))
    y_a = _matmul(a_bf, P["w_attn_out"], "nn", "attn_out")
    c1 = _conv_fwd_call(proj, glu_col0, P["conv_dw_w"], P["conv_dw_b"])
    cact = _ln_silu_call(c1, P["conv_ln_g"], P["conv_ln_b"])
    y_c = _matmul(cact, P["conv_pw_w"], "nn", "conv_pw")
    mixed = _mix_call(proj, gate_col0, P["b_gate"], y_a, y_c)
    out = _matmul(mixed, P["w_out"], "nn", "mix_out")
    x1, h2 = _res1_call(xs, out, P["norm_mix_post"], P["norm_ffn_pre"])
    P = dict(P, **late_weights("up", h2))
    u = _matmul(h2, P["w_up"], "nn", "ffn_up")
    f = _ffn_fwd_call(u, P["ffn_conv_w"], P["ffn_conv_b"])
    P = dict(P, **late_weights("down", f))
    yff = _matmul(f, P["w_down"], "nn", "ffn_down")
    loss_tile, dx2, dyff, dg_ffn_post = _loss_call(yff, x1, P["norm_ffn_post"], target)

    G = {}
    G["norm_ffn_post"] = dg_ffn_post
    zero = on_grad("w_down", shard_major(_matmul(f, dyff, "tn", "ffn_down_dw")))
    df = _matmul(dyff, P["w_down"], "nt", "ffn_down_dx")
    du, dwg, dwv, dbg, dbv = _ffn_bwd_call(u, P["ffn_conv_w"], P["ffn_conv_b"] + zero, df)
    G["ffn_conv_w"] = jnp.concatenate([dwg[:FFN_CONV_WIDTH], dwv[:FFN_CONV_WIDTH]], axis=1)
    G["ffn_conv_b"] = jnp.concatenate([dbg, dbv], axis=1)
    zero = on_grad("w_up", _matmul(h2, du, "tn", "ffn_up_dw", out_shards=True))
    dh2 = _matmul(du, P["w_up"], "nt", "ffn_up_dx")
    dx1, dout, G["norm_ffn_pre"], G["norm_mix_post"] = _mid_bwd_call(x1, P["norm_ffn_pre"] + zero, dh2, dx2, out, P["norm_mix_post"])
    zero = on_grad("w_out", shard_major(_matmul(mixed, dout, "tn", "mix_out_dw")))
    dmixed = _matmul(dout, P["w_out"], "nt", "mix_out_dx")
    dya, dyc, dga, dgc, dba, dbc = _mix_bwd_call(dmixed, proj, gate_col0, P["b_gate"] + zero, y_a, y_c)
    G["b_gate"] = jnp.concatenate([dba, dbc], axis=1)
    zero = on_grad("w_attn_out", _matmul(a_bf, dya, "tn", "attn_out_dw", out_shards=True))
    zero = zero + on_grad("conv_pw_w", shard_major(_matmul(cact, dyc, "tn", "conv_pw_dw")))
    da = _matmul(dya, P["w_attn_out"], "nt", "attn_out_dx")
    dcact = _matmul(dyc, P["conv_pw_w"], "nt", "conv_pw_dx")
    dc1, G["conv_ln_g"], G["conv_ln_b"] = _ln_silu_bwd_call(c1, P["conv_ln_g"] + zero, P["conv_ln_b"], dcact)
    dval, dgate, dw_dw, G["conv_dw_b"] = _conv_bwd_call(proj, glu_col0, P["conv_dw_w"], dc1)
    G["conv_dw_w"] = dw_dw[:CONV_WIDTH]
    delta = _attn_delta_call(a, da)
    dqs, dks, dvs, dbs = [], [], [], []
    for g in range(N_GROUPS):
        dq, dk, dv, db = _attn_bwd_call(proj, bias, da, lse, delta, g)
        dqs.append(dq)
        dks.append(dk)
        dvs.append(dv)
        dbs.append(db)
    G["rel_bias"] = _bias_grad_call(jnp.concatenate(dbs, axis=0), buckets)
    dproj = _dproj_call(dqs + dks + dvs, [dval, dgate, dga, dgc])
    zero = on_grad("w_in", _matmul(h1, dproj, "tn", "proj_in_dw", out_shards=True))
    dh1 = _matmul(dproj, P["w_in"], "nt", "proj_in_dx")
    zero = zero + on_grad(None, dh1)
    grad_x, G["norm_mix_pre"] = _in_bwd_call(xs, P["norm_mix_pre"] + zero, dh1, dx1)
    return loss_tile, grad_x, G


def kernel(x, w_in, b_gate, rel_bias, w_attn_out, conv_dw_w, conv_dw_b, conv_ln_g, conv_ln_b, conv_pw_w, w_out, norm_mix_pre, norm_mix_post, norm_ffn_pre, norm_ffn_post, w_up, ffn_conv_w, ffn_conv_b, w_down, loss_target, m_w_in, m_b_gate, m_rel_bias, m_w_attn_out, m_conv_dw_w, m_conv_dw_b, m_conv_ln_g, m_conv_ln_b, m_conv_pw_w, m_w_out, m_norm_mix_pre, m_norm_mix_post, m_norm_ffn_pre, m_norm_ffn_post, m_w_up, m_ffn_conv_w, m_ffn_conv_b, m_w_down, v_w_in, v_b_gate, v_rel_bias, v_w_attn_out, v_conv_dw_w, v_conv_dw_b, v_conv_ln_g, v_conv_ln_b, v_conv_pw_w, v_w_out, v_norm_mix_pre, v_norm_mix_post, v_norm_ffn_pre, v_norm_ffn_post, v_w_up, v_ffn_conv_w, v_ffn_conv_b, v_w_down):
    weights = dict(w_in=w_in, b_gate=b_gate, rel_bias=rel_bias, w_attn_out=w_attn_out, conv_dw_w=conv_dw_w, conv_dw_b=conv_dw_b,
                   conv_ln_g=conv_ln_g, conv_ln_b=conv_ln_b, conv_pw_w=conv_pw_w, w_out=w_out, norm_mix_pre=norm_mix_pre,
                   norm_mix_post=norm_mix_post, norm_ffn_pre=norm_ffn_pre, norm_ffn_post=norm_ffn_post, w_up=w_up,
                   ffn_conv_w=ffn_conv_w, ffn_conv_b=ffn_conv_b, w_down=w_down)
    m_in = dict(w_in=m_w_in, b_gate=m_b_gate, rel_bias=m_rel_bias, w_attn_out=m_w_attn_out, conv_dw_w=m_conv_dw_w,
                conv_dw_b=m_conv_dw_b, conv_ln_g=m_conv_ln_g, conv_ln_b=m_conv_ln_b, conv_pw_w=m_conv_pw_w, w_out=m_w_out,
                norm_mix_pre=m_norm_mix_pre, norm_mix_post=m_norm_mix_post, norm_ffn_pre=m_norm_ffn_pre,
                norm_ffn_post=m_norm_ffn_post, w_up=m_w_up, ffn_conv_w=m_ffn_conv_w, ffn_conv_b=m_ffn_conv_b, w_down=m_w_down)
    v_in = dict(w_in=v_w_in, b_gate=v_b_gate, rel_bias=v_rel_bias, w_attn_out=v_w_attn_out, conv_dw_w=v_conv_dw_w,
                conv_dw_b=v_conv_dw_b, conv_ln_g=v_conv_ln_g, conv_ln_b=v_conv_ln_b, conv_pw_w=v_conv_pw_w, w_out=v_w_out,
                norm_mix_pre=v_norm_mix_pre, norm_mix_post=v_norm_mix_post, norm_ffn_pre=v_norm_ffn_pre,
                norm_ffn_post=v_norm_ffn_post, w_up=v_w_up, ffn_conv_w=v_ffn_conv_w, ffn_conv_b=v_ffn_conv_b, w_down=v_w_down)
    names = list(weights)
    xi, yi, ci = _position()
    chip = 2 * xi + yi
    core_arr = jnp.reshape(ci, (1,)).astype(jnp.int32)

    xs = x[0]
    target = loss_target[0]
    S, D = xs.shape

    big = ["w_in", "w_attn_out", "conv_pw_w", "w_out", "w_up", "w_down"]
    row_sharded = ("conv_pw_w", "w_out", "w_down")
    bf16_shard = {k: weights[k][0].astype(BF16) for k in big}
    natural = lambda k, g: g.reshape(-1, g.shape[2]) if k in row_sharded else g
    first_srcs = [bf16_shard["w_in"], conv_dw_w[0], ffn_conv_w[0]]
    first_lands = [lax.empty((N_CHIPS,) + s.shape, s.dtype) for s in first_srcs]
    (first_hop,) = _split_start("gather_in_start", [(first_srcs, first_lands, 4 * len(first_srcs), _first_hop_copies)], core_arr)
    launched = first_hop["token"]
    late_sets = dict(mix=["w_attn_out", "conv_pw_w", "w_out"], up=["w_up"], down=["w_down"])
    late_groups = []
    for keys in late_sets.values():
        srcs = [bf16_shard[k] for k in keys]
        late_groups.append((srcs, [lax.empty((N_CHIPS,) + s.shape, BF16) for s in srcs], 4 * len(keys), _gather_copies))
    started = {}

    def late_weights(tag, after):
        if tag == "in":
            w_in_halves, dw4, fc4 = _split_wait("gather_in_wait", first_hop, _first_hop_copies, after)[len(first_srcs):]
            (second_hop,) = _split_start("gather_in_pass_start", [([], [w_in_halves], 3, _second_hop_copies)], dw4)
            (w_in_full,) = _split_wait("gather_in_pass_wait", second_hop, _second_hop_copies, second_hop["tile"])
            started.update(zip(late_sets, _split_start("gather_late_start", late_groups, dw4, carry=w_in_full)))
            return dict(w_in=started["mix"]["carry"], conv_dw_w=jnp.concatenate(list(dw4), axis=1), ffn_conv_w=jnp.concatenate(list(fc4), axis=1))
        landed = _split_wait(f"gather_{tag}_wait", started[tag], _gather_copies, after)[len(late_sets[tag]):]
        return {k: natural(k, g) for k, g in zip(late_sets[tag], landed)}

    chip_core = jnp.stack([chip, ci]).astype(jnp.int32)
    exchanging, pending = {}, {}

    def launch(tag, g3, after):
        keys, groups, partial = [], [], {}
        for k in list(exchanging):
            gk, r1 = _split_wait(f"sibling_exchange_wait_{k}", exchanging.pop(k), _sibling_copies, after)
            partial[k], s16 = _pair_sum_call(gk, r1, core_arr, f"pair_sum_{k}")
            keys.append(k)
            groups.append(([s16], [lax.empty((3,) + s16.shape[1:], BF16)], 3, _exchange_copies))
        if g3 is not None:
            groups.append(([g3], [lax.empty((N_CHIPS, g3.shape[1] // 2, g3.shape[2]), F32)], 1, _sibling_copies))
        begun = _split_start(f"grad_exchange_start_{tag}", groups, core_arr)
        for k, st in zip(keys, begun):
            pending[k] = (partial[k], st)
        if g3 is not None:
            exchanging[tag] = begun[-1]
        return begun[0]["token"]

    def on_grad(k, g3):
        if k is None:
            return launch("last", None, g3[:SUBLANES, :LANES])
        return launch(k, g3, g3[0, :SUBLANES, :LANES])

    def finish(keys, after, tag):
        halves = []
        for k in keys:
            s32, st = pending[k]
            recv2 = _split_wait(f"chip_exchange_wait_{k}", st, _exchange_copies, after)[1]
            halves.append(_chip_sum_call(s32, recv2, chip_core, f"chip_sum_{k}"))
        return dict(zip(keys, _sibling_assemble_call(halves, f"grad_sibling_assemble_{tag}")))

    P = dict(b_gate=b_gate, rel_bias=rel_bias, conv_dw_b=conv_dw_b, conv_ln_g=conv_ln_g, conv_ln_b=conv_ln_b,
             norm_mix_pre=norm_mix_pre + launched, norm_mix_post=norm_mix_post, norm_ffn_pre=norm_ffn_pre,
             norm_ffn_post=norm_ffn_post, ffn_conv_b=ffn_conv_b)
    loss_tile, grad_x, G = _local_step(xs, target, P, late_weights, on_grad)

    small = [k for k in names if k not in big]
    packed = _pack([loss_tile[:1]] + [G[k] for k in small])
    (allsum,) = _split_start("small_grad_allsum_start",
                             [([packed], [jnp.zeros((N_DEVICES,) + packed.shape, F32)], N_DEVICES - 1, _allsum_copies)], core_arr)

    reduced, grads, deltas, new_m, new_v = {}, {}, {}, {}, {}

    def update(keys):
        for k in keys:
            gk, d, mn, vn = _adamw_call(weights[k][0], reduced[k], m_in[k][0], v_in[k][0], f"adamw_{k}")
            grads[k], deltas[k], new_m[k], new_v[k] = gk[None], d[None], mn[None], vn[None]

    others = [k for k in big if k != "w_in"]
    reduced.update(finish(others, allsum["tile"], "others"))
    update(others)
    reduced.update(finish(["w_in"], deltas["w_up"], "w_in"))
    update(["w_in"])

    me = jnp.reshape(4 * xi + 2 * yi + ci, (1,)).astype(jnp.int32)
    mine, landed = _split_wait("small_grad_allsum_wait", allsum, _allsum_copies, deltas["w_in"])
    summed_block = _ordered_sum_call(mine, landed, me)
    loss_row, *summed = _unpack(summed_block, [(1, LANES)] + [G[k].shape for k in small])
    loss = loss_row[0, 0]
    for k, gsum in zip(small, summed):
        if k in ("conv_dw_w", "ffn_conv_w"):
            cols = weights[k].shape[2]
            reduced[k] = lax.dynamic_slice_in_dim(gsum, chip * cols, cols, axis=1)
        else:
            reduced[k] = gsum
    for k in small:
        grads[k] = reduced[k].reshape(weights[k].shape)
    ds, mns, vns = _adamw_small_call([weights[k] for k in small], [grads[k] for k in small],
                                     [m_in[k] for k in small], [v_in[k] for k in small])
    deltas.update(zip(small, ds))
    new_m.update(zip(small, mns))
    new_v.update(zip(small, vns))

    return (loss, grad_x[None], *[grads[k] for k in names], *[deltas[k] for k in names],
            *[new_m[k] for k in names], *[new_v[k] for k in names])
```

```python
import functools
import math

import jax
import jax.numpy as jnp
import numpy as np
from jax import lax
from jax.experimental import pallas as pl
from jax.experimental.pallas import tpu as pltpu

F32 = jnp.float32
BF16 = jnp.bfloat16
MESH = pl.DeviceIdType.MESH

HEAD_DIM = 128
HEADS_PER_GROUP = 4
DILATED_PATTERNS = ((128, 1), (512, 4), (2048, 16))
N_GROUPS = 3
N_HEADS = N_GROUPS * HEADS_PER_GROUP
SPAN = 128
GROUP_WIDTH = HEADS_PER_GROUP * HEAD_DIM
CONV_WIDTH = 31
FFN_CONV_WIDTH = 3
N_BUCKETS = 32
MAX_DISTANCE = 2048
RMS_EPS = 1e-6
LN_EPS = 1e-5
NEG_INF = -1e30
ADAM_LR = 0.001
ADAM_B1 = 0.9
ADAM_B2 = 0.999
ADAM_EPS = 1e-08
ADAM_WD = 0.01
ADAM_STEP = 10

LANES = 128
SUBLANES = 8
ROW_TILE = 512
GATE_ROWS, GATE_COLS = 512, 512
TIME_BLOCK = 128
CONV_PAD = 32
FFN_PAD = 8
VMEM_LIMIT = 56 << 20


def _params(sem=None, vmem=None):
    kw = {}
    if sem is not None:
        kw["dimension_semantics"] = sem
    if vmem is not None:
        kw["vmem_limit_bytes"] = vmem
    return pltpu.CompilerParams(**kw)


def _pick(n, cands):
    for c in cands:
        if n % c == 0:
            return c
    return n


ELEMENTWISE_TILE_BYTES = 3 << 19


def _row_tile(rows, cols):
    for align in (16, SUBLANES):
        fits = [t for t in range(align, rows + 1, align) if rows % t == 0 and t * cols * 4 <= ELEMENTWISE_TILE_BYTES]
        if fits:
            return max(fits)
    return SUBLANES


N_CHIPS = 4
M_TILES = (1024, 1408, 512, 256, 128)
N_TILES = (1024, 512, 1408, 256, 128)
K_TILES = (2176, 2048, 1408, 1024, 512, 256, 128)


def _matmul(a, b, mode, name, out_shards=False, tm=None):
    assert a.dtype == BF16 and b.dtype == BF16, (name, a.dtype, b.dtype)
    b3 = b.ndim == 3
    tn = tk = None
    halves = None
    if mode == "nn":
        M, K = a.shape
        N = b.shape[-1] * (N_CHIPS if b3 else 1)
        tn = b.shape[-1] if b3 else None
    elif mode == "nt":
        if a.ndim == 3:
            halves = a.shape[2]
        M, K = a.shape[-2], a.shape[-1] * (a.shape[0] if a.ndim == 3 else 1)
        N = b.shape[-2]
        tk = b.shape[-1] if b3 else None
    else:
        if b3:
            halves = b.shape[2]
        K, M = a.shape
        N = b.shape[-1] * (b.shape[0] if b3 else 1)
        tn = N // N_CHIPS if out_shards else None
    tm = tm or _pick(M, M_TILES)
    tn = tn or _pick(N, N_TILES)
    tk = tk or _pick(K, K_TILES)
    nk = K // tk
    dn = {"nn": (((1,), (0,)), ((), ())), "nt": (((1,), (1,)), ((), ())), "tn": (((0,), (0,)), ((), ()))}[mode]

    def body(a_ref, b_ref, o_ref):
        if nk == 1:
            o_ref[...] = lax.dot_general(a_ref[...], b_ref[...], dn, preferred_element_type=F32)
        else:
            @pl.when(pl.program_id(2) == 0)
            def _():
                o_ref[...] = jnp.zeros_like(o_ref)

            o_ref[...] += lax.dot_general(a_ref[...], b_ref[...], dn, preferred_element_type=F32)

    if mode == "tn":
        a_spec = pl.BlockSpec((tk, tm), lambda i, j, k: (k, i))
    elif halves:
        per = halves // tk
        a_spec = pl.BlockSpec((None, tm, tk), lambda i, j, k: (k // per, i, k % per))
    else:
        a_spec = pl.BlockSpec((tm, tk), lambda i, j, k: (i, k))
    if mode == "nn":
        b_spec = pl.BlockSpec((None, tk, tn), lambda i, j, k: (j, k, 0)) if b3 else pl.BlockSpec((tk, tn), lambda i, j, k: (k, j))
    elif mode == "nt":
        b_spec = pl.BlockSpec((None, tn, tk), lambda i, j, k: (k, j, 0)) if b3 else pl.BlockSpec((tn, tk), lambda i, j, k: (j, k))
    elif halves:
        per = halves // tn
        b_spec = pl.BlockSpec((None, tk, tn), lambda i, j, k: (j // per, k, j % per))
    else:
        b_spec = pl.BlockSpec((tk, tn), lambda i, j, k: (k, j))
    if out_shards:
        out_spec = pl.BlockSpec((None, tm, tn), lambda i, j, k: (j, i, 0))
        out_shape = jax.ShapeDtypeStruct((N_CHIPS, M, tn), F32)
    else:
        out_spec = pl.BlockSpec((tm, tn), lambda i, j, k: (i, j))
        out_shape = jax.ShapeDtypeStruct((M, N), F32)
    return pl.pallas_call(
        body, name=name, grid=(M // tm, N // tn, nk),
        in_specs=[a_spec, b_spec], out_specs=out_spec, out_shape=out_shape,
        compiler_params=_params(("parallel", "parallel", "arbitrary"), VMEM_LIMIT),
    )(a, b)


def _rms(x, g):
    r = lax.rsqrt(jnp.mean(x * x, axis=-1, keepdims=True) + RMS_EPS)
    return x * r * g


def _rms_bwd(x, g, dy):
    r = lax.rsqrt(jnp.mean(x * x, axis=-1, keepdims=True) + RMS_EPS)
    n = x * r
    dn = dy * g
    dx = r * (dn - n * jnp.mean(dn * n, axis=-1, keepdims=True))
    return dx, jnp.sum(dy * n, axis=0, keepdims=True)


def _sigmoid(x):
    return 1.0 / (1.0 + jnp.exp(-x))


_GELU_C = math.sqrt(2.0 / math.pi)


def _gelu(x):
    return 0.5 * x * (1.0 + jnp.tanh(_GELU_C * (x + 0.044715 * x * x * x)))


def _gelu_and_grad(x):
    x2 = x * x
    t = jnp.tanh(_GELU_C * x * (1.0 + 0.044715 * x2))
    half = 0.5 * (1.0 + t)
    return x * half, half + (0.5 * _GELU_C) * x * (1.0 - t * t) * (1.0 + (3.0 * 0.044715) * x2)


def _row_spec(width, col_block=0):
    return pl.BlockSpec((ROW_TILE, width), lambda i: (i, col_block))


def _vec_spec(width, col_block=0):
    return pl.BlockSpec((1, width), lambda i: (0, col_block))


def _accumulate(ref, part):
    @pl.when(pl.program_id(0) == 0)
    def _():
        ref[...] = part

    @pl.when(pl.program_id(0) > 0)
    def _():
        ref[...] += part


def _rms_fwd_call(x, g):
    S, D = x.shape

    def body(x_ref, g_ref, h_ref):
        h_ref[...] = _rms(x_ref[...], g_ref[...]).astype(BF16)

    return pl.pallas_call(
        body, name="rms_mix_pre", grid=(S // ROW_TILE,),
        in_specs=[_row_spec(D), _vec_spec(D)], out_specs=_row_spec(D),
        out_shape=jax.ShapeDtypeStruct((S, D), BF16),
        compiler_params=_params(("parallel",)),
    )(x, g)


def _ln_silu_call(c1, g, b):
    S, C = c1.shape

    def body(c_ref, g_ref, b_ref, o_ref):
        xv = c_ref[...]
        mu = jnp.mean(xv, axis=-1, keepdims=True)
        xc = xv - mu
        var = jnp.mean(xc * xc, axis=-1, keepdims=True)
        z = xc * lax.rsqrt(var + LN_EPS) * g_ref[...] + b_ref[...]
        o_ref[...] = (z * _sigmoid(z)).astype(BF16)

    return pl.pallas_call(
        body, name="conv_ln_silu", grid=(S // ROW_TILE,),
        in_specs=[_row_spec(C), _vec_spec(C), _vec_spec(C)], out_specs=_row_spec(C),
        out_shape=jax.ShapeDtypeStruct((S, C), BF16),
        compiler_params=_params(("parallel",)),
    )(c1, g, b)


def _ln_silu_bwd_call(c1, g, b, dc):
    S, C = c1.shape

    def body(c_ref, g_ref, b_ref, dc_ref, dx_ref, dg_ref, db_ref):
        xv = c_ref[...]
        mu = jnp.mean(xv, axis=-1, keepdims=True)
        xc = xv - mu
        rs = lax.rsqrt(jnp.mean(xc * xc, axis=-1, keepdims=True) + LN_EPS)
        xh = xc * rs
        z = xh * g_ref[...] + b_ref[...]
        sg = _sigmoid(z)
        dz = dc_ref[...] * (sg * (1.0 + z * (1.0 - sg)))
        dxh = dz * g_ref[...]
        dx_ref[...] = rs * (dxh - jnp.mean(dxh, axis=-1, keepdims=True) - xh * jnp.mean(dxh * xh, axis=-1, keepdims=True))
        _accumulate(dg_ref, jnp.sum(dz * xh, axis=0, keepdims=True))
        _accumulate(db_ref, jnp.sum(dz, axis=0, keepdims=True))

    return pl.pallas_call(
        body, name="conv_ln_silu_bwd", grid=(S // ROW_TILE,),
        in_specs=[_row_spec(C), _vec_spec(C), _vec_spec(C), _row_spec(C)],
        out_specs=[_row_spec(C), _vec_spec(C), _vec_spec(C)],
        out_shape=[jax.ShapeDtypeStruct((S, C), F32), jax.ShapeDtypeStruct((1, C), F32), jax.ShapeDtypeStruct((1, C), F32)],
        compiler_params=_params(("arbitrary",)),
    )(c1, g, b, dc)


def _mix_call(proj, gate_col0, b_gate, y_a, y_c):
    S, D = y_a.shape
    w = GATE_COLS
    nc = D // w
    ga0, gc0 = gate_col0 // w, (gate_col0 + D) // w

    def body(ga_ref, gc_ref, ba_ref, bc_ref, ya_ref, yc_ref, o_ref):
        o_ref[...] = (_sigmoid(ga_ref[...] + ba_ref[...]) * ya_ref[...]
                      + _sigmoid(gc_ref[...] + bc_ref[...]) * yc_ref[...]).astype(BF16)

    tile = lambda off: pl.BlockSpec((GATE_ROWS, w), lambda i, j: (i, off + j))
    vec = lambda off: pl.BlockSpec((1, w), lambda i, j: (0, off + j))
    return pl.pallas_call(
        body, name="gate_mix", grid=(S // GATE_ROWS, nc),
        in_specs=[tile(ga0), tile(gc0), vec(0), vec(nc), tile(0), tile(0)],
        out_specs=tile(0), out_shape=jax.ShapeDtypeStruct((S, D), BF16),
        compiler_params=_params(("parallel", "parallel")),
    )(proj, proj, b_gate, b_gate, y_a, y_c)


def _mix_bwd_call(dmixed, proj, gate_col0, b_gate, y_a, y_c):
    S, D = y_a.shape
    w = GATE_COLS
    nc = D // w
    ga0, gc0 = gate_col0 // w, (gate_col0 + D) // w

    def body(dm_ref, ga_ref, gc_ref, ba_ref, bc_ref, ya_ref, yc_ref, dya_ref, dyc_ref, dga_ref, dgc_ref, dba_ref, dbc_ref):
        dm = dm_ref[...]
        sa = _sigmoid(ga_ref[...] + ba_ref[...])
        sc = _sigmoid(gc_ref[...] + bc_ref[...])
        dya_ref[...] = (dm * sa).astype(BF16)
        dyc_ref[...] = (dm * sc).astype(BF16)
        dga = dm * ya_ref[...] * sa * (1.0 - sa)
        dgc = dm * yc_ref[...] * sc * (1.0 - sc)
        dga_ref[...] = dga.astype(BF16)
        dgc_ref[...] = dgc.astype(BF16)
        pa = jnp.sum(dga, axis=0, keepdims=True)
        pc = jnp.sum(dgc, axis=0, keepdims=True)

        @pl.when(pl.program_id(1) == 0)
        def _():
            dba_ref[...] = pa
            dbc_ref[...] = pc

        @pl.when(pl.program_id(1) > 0)
        def _():
            dba_ref[...] += pa
            dbc_ref[...] += pc

    tile = lambda off: pl.BlockSpec((GATE_ROWS, w), lambda j, i: (i, off + j))
    vec = lambda off: pl.BlockSpec((1, w), lambda j, i: (0, off + j))
    return pl.pallas_call(
        body, name="gate_mix_bwd", grid=(nc, S // GATE_ROWS),
        in_specs=[tile(0), tile(ga0), tile(gc0), vec(0), vec(nc), tile(0), tile(0)],
        out_specs=[tile(0), tile(0), tile(0), tile(0), vec(0), vec(0)],
        out_shape=[jax.ShapeDtypeStruct((S, D), BF16)] * 4 + [
                   jax.ShapeDtypeStruct((1, D), F32), jax.ShapeDtypeStruct((1, D), F32)],
        compiler_params=_params(("parallel", "arbitrary")),
    )(dmixed, proj, proj, b_gate, b_gate, y_a, y_c)


def _res1_call(x, out, g_post, g_pre):
    S, D = x.shape

    def body(x_ref, o_ref, gp_ref, gq_ref, x1_ref, h2_ref):
        x1 = x_ref[...] + _rms(o_ref[...], gp_ref[...])
        x1_ref[...] = x1
        h2_ref[...] = _rms(x1, gq_ref[...]).astype(BF16)

    return pl.pallas_call(
        body, name="residual_mix", grid=(S // ROW_TILE,),
        in_specs=[_row_spec(D), _row_spec(D), _vec_spec(D), _vec_spec(D)],
        out_specs=[_row_spec(D), _row_spec(D)],
        out_shape=[jax.ShapeDtypeStruct((S, D), F32), jax.ShapeDtypeStruct((S, D), BF16)],
        compiler_params=_params(("parallel",)),
    )(x, out, g_post, g_pre)


def _loss_call(y, x1, g_post, target):
    S, D = y.shape

    def body(y_ref, x1_ref, g_ref, t_ref, loss_ref, dx_ref, dy_ref, dg_ref):
        yv, gv = y_ref[...], g_ref[...]
        err = x1_ref[...] + _rms(yv, gv) - t_ref[...]
        dx2 = err * (1.0 / D)
        dx_ref[...] = dx2
        dy, dg = _rms_bwd(yv, gv, dx2)
        dy_ref[...] = dy.astype(BF16)
        _accumulate(dg_ref, dg)
        part = 0.5 * jnp.sum(jnp.mean(err * err, axis=-1, keepdims=True), axis=0, keepdims=True)
        _accumulate(loss_ref, jnp.broadcast_to(part, (SUBLANES, LANES)))

    return pl.pallas_call(
        body, name="residual_ffn_loss", grid=(S // ROW_TILE,),
        in_specs=[_row_spec(D), _row_spec(D), _vec_spec(D), _row_spec(D)],
        out_specs=[pl.BlockSpec((SUBLANES, LANES), lambda i: (0, 0)), _row_spec(D), _row_spec(D), _vec_spec(D)],
        out_shape=[jax.ShapeDtypeStruct((SUBLANES, LANES), F32), jax.ShapeDtypeStruct((S, D), F32),
                   jax.ShapeDtypeStruct((S, D), BF16), jax.ShapeDtypeStruct((1, D), F32)],
        compiler_params=_params(("arbitrary",)),
    )(y, x1, g_post, target)


def _mid_bwd_call(x1, g_pre, dh2, dx2, out, g_post):
    S, D = x1.shape

    def body(x1_ref, gq_ref, dh_ref, dx2_ref, o_ref, gp_ref, dx1_ref, do_ref, dgq_ref, dgp_ref):
        d, dgq = _rms_bwd(x1_ref[...], gq_ref[...], dh_ref[...])
        dx1 = dx2_ref[...] + d
        dx1_ref[...] = dx1
        do, dgp = _rms_bwd(o_ref[...], gp_ref[...], dx1)
        do_ref[...] = do.astype(BF16)
        _accumulate(dgq_ref, dgq)
        _accumulate(dgp_ref, dgp)

    return pl.pallas_call(
        body, name="residual_mix_bwd", grid=(S // ROW_TILE,),
        in_specs=[_row_spec(D), _vec_spec(D), _row_spec(D), _row_spec(D), _row_spec(D), _vec_spec(D)],
        out_specs=[_row_spec(D), _row_spec(D), _vec_spec(D), _vec_spec(D)],
        out_shape=[jax.ShapeDtypeStruct((S, D), F32), jax.ShapeDtypeStruct((S, D), BF16)] + [jax.ShapeDtypeStruct((1, D), F32)] * 2,
        compiler_params=_params(("arbitrary",)),
    )(x1, g_pre, dh2, dx2, out, g_post)


def _in_bwd_call(x, g, dh1, dx1):
    S, D = x.shape

    def body(x_ref, g_ref, dh_ref, dx1_ref, gx_ref, dg_ref):
        d, dg = _rms_bwd(x_ref[...], g_ref[...], dh_ref[...])
        gx_ref[...] = dx1_ref[...] + d
        _accumulate(dg_ref, dg)

    return pl.pallas_call(
        body, name="rms_mix_pre_bwd", grid=(S // ROW_TILE,),
        in_specs=[_row_spec(D), _vec_spec(D), _row_spec(D), _row_spec(D)],
        out_specs=[_row_spec(D), _vec_spec(D)],
        out_shape=[jax.ShapeDtypeStruct((S, D), F32), jax.ShapeDtypeStruct((1, D), F32)],
        compiler_params=_params(("arbitrary",)),
    )(x, g, dh1, dx1)


def _bucket_table(dilation):
    qi = np.arange(SPAN)[:, None]
    ki = np.arange(2 * SPAN)[None, :]
    dist = np.maximum(qi + SPAN - ki, 0) * dilation
    max_exact = N_BUCKETS // 2
    d = np.maximum(dist, 1).astype(np.float64)
    large = max_exact + (np.log(d / max_exact) / math.log(MAX_DISTANCE / max_exact) * (N_BUCKETS - max_exact)).astype(np.int32)
    large = np.minimum(large, N_BUCKETS - 1)
    return np.where(dist < max_exact, dist, large).astype(np.int32)


def _bucket_tables():
    return jnp.asarray(np.stack([_bucket_table(r) for _, r in DILATED_PATTERNS]))


def _bias_table_call(rel_bias, buckets):
    def body(rb_ref, bk_ref, o_ref):
        for h in range(N_HEADS):
            bk = bk_ref[h // HEADS_PER_GROUP]

            def step(b, acc):
                return jnp.where(bk == b, rb_ref[b, h], acc)

            o_ref[h] = lax.fori_loop(0, N_BUCKETS, step, jnp.zeros((SPAN, 2 * SPAN), F32))

    return pl.pallas_call(
        body, name="rel_bias_table",
        in_specs=[pl.BlockSpec(memory_space=pltpu.SMEM), pl.BlockSpec(memory_space=pltpu.VMEM)],
        out_specs=pl.BlockSpec(memory_space=pltpu.VMEM),
        out_shape=jax.ShapeDtypeStruct((N_HEADS, SPAN, 2 * SPAN), F32),
    )(rel_bias, buckets)


def _bias_grad_call(dbias, buckets):
    def body(db_ref, bk_ref, o_ref, rows_ref):
        for h in range(N_HEADS):
            bk = bk_ref[h // HEADS_PER_GROUP]
            dv = db_ref[h]

            def step(b, carry):
                rows_ref[h, b] = jnp.sum(jnp.where(bk == b, dv, 0.0), axis=0, keepdims=True)
                return carry

            lax.fori_loop(0, N_BUCKETS, step, 0)
        o_ref[...] = jnp.sum(rows_ref[...], axis=-1, keepdims=True)

    out = pl.pallas_call(
        body, name="rel_bias_grad",
        in_specs=[pl.BlockSpec(memory_space=pltpu.VMEM), pl.BlockSpec(memory_space=pltpu.VMEM)],
        out_specs=pl.BlockSpec(memory_space=pltpu.VMEM),
        out_shape=jax.ShapeDtypeStruct((N_HEADS, N_BUCKETS, 1, 1), F32),
        scratch_shapes=[pltpu.VMEM((N_HEADS, N_BUCKETS, 1, 2 * SPAN), F32)],
    )(dbias, buckets)
    return out.reshape(N_HEADS, N_BUCKETS).T


def _dot_nt(a, b):
    return lax.dot_general(a, b, (((1,), (1,)), ((), ())), preferred_element_type=F32)


def _dot_nn(a, b):
    return lax.dot_general(a, b, (((1,), (0,)), ((), ())), preferred_element_type=F32)


def _dot_tn(a, b):
    return lax.dot_general(a, b, (((0,), (0,)), ((), ())), preferred_element_type=F32)


def _band_masks(n, nb):
    qi = lax.broadcasted_iota(jnp.int32, (SPAN, SPAN), 0)
    ki = lax.broadcasted_iota(jnp.int32, (SPAN, SPAN), 1)
    prev_ok = jnp.logical_and(ki >= qi, n > 0)
    cur_ok = ki <= qi
    next_ok = jnp.logical_and(ki >= qi, n < nb - 1)
    return prev_ok, cur_ok, next_ok


def _wide_band_mask(n):
    qi = lax.broadcasted_iota(jnp.int32, (SPAN, 2 * SPAN), 0)
    ki = lax.broadcasted_iota(jnp.int32, (SPAN, 2 * SPAN), 1)
    prev_ok = jnp.logical_and(jnp.logical_and(ki < SPAN, ki >= qi), n > 0)
    cur_ok = jnp.logical_and(ki >= SPAN, ki - SPAN <= qi)
    return jnp.logical_or(prev_ok, cur_ok)


def _attn_plan(S, group):
    r = DILATED_PATTERNS[group][1]
    hp, per = (HEADS_PER_GROUP, 1) if r == 1 else (2, 4)
    return r, S // (r * SPAN), hp, per


def _residue_rows(rho, r):
    return slice(None) if r == 1 else pl.ds(rho, SPAN, stride=r)


def _for_residues(r, per, fn):
    if r == per:
        for u in range(per):
            fn(u)
        return

    def step(i, carry):
        for u in range(per):
            fn(i * per + u)
        return carry

    lax.fori_loop(0, r // per, step, 0)


def _attn_fwd_call(proj, bias, group):
    S = proj.shape[0]
    r, nb, hp, per = _attn_plan(S, group)
    scale = HEAD_DIM ** -0.5
    kinds = ("q", "kp", "kc", "vp", "vc") if nb > 1 else ("q", "kc", "vc")

    def body(*refs):
        ins = {kind: refs[i * hp:(i + 1) * hp] for i, kind in enumerate(kinds)}
        b_ref, o_ref, lse_ref = refs[len(kinds) * hp:]
        n = pl.program_id(1)
        prev_ok, cur_ok, _ = _band_masks(n, nb)

        band_ok = _wide_band_mask(n) if nb > 1 else cur_ok

        def residue(rho):
            rows = _residue_rows(rho, r)
            for j in range(hp):
                get = lambda kind: ins[kind][j][rows, :].astype(BF16)
                q = get("q")
                if nb > 1:
                    keys, vals, bias_j = jnp.concatenate([get("kp"), get("kc")], axis=0), jnp.concatenate([get("vp"), get("vc")], axis=0), b_ref[j]
                else:
                    keys, vals, bias_j = get("kc"), get("vc"), b_ref[j, :, SPAN:]
                s = jnp.where(band_ok, _dot_nt(q, keys) * scale + bias_j, NEG_INF)
                m = jnp.max(s, axis=-1, keepdims=True)
                p = jnp.exp(s - m)
                den = jnp.sum(p, axis=-1, keepdims=True)
                o_ref[j, rows, :] = _dot_nn(p.astype(BF16), vals) / den
                lse_ref[j, rows, :] = jnp.broadcast_to(m + jnp.log(den), (SPAN, HEAD_DIM))

        _for_residues(r, per, residue)

    in_specs = [_head_spec(r, nb, hp, kind, group, jj) for kind in kinds for jj in range(hp)]
    in_specs.append(pl.BlockSpec((hp, SPAN, 2 * SPAN), lambda j, n: (group * (HEADS_PER_GROUP // hp) + j, 0, 0)))
    out = pl.BlockSpec((hp, r * SPAN, HEAD_DIM), lambda j, n: (j, n, 0))
    return pl.pallas_call(
        body, name=f"attn_fwd_g{group}", grid=(HEADS_PER_GROUP // hp, nb),
        in_specs=in_specs, out_specs=[out] * 2,
        out_shape=[jax.ShapeDtypeStruct((HEADS_PER_GROUP, S, HEAD_DIM), F32)] * 2,
        compiler_params=_params(("parallel", "parallel"), VMEM_LIMIT),
    )(*([proj] * (len(in_specs) - 1)), bias)


_PROJ_PART = dict(q=0, qn=0, kp=1, kc=1, vp=2, vc=2)


def _head_spec(r, nb, hp, kind, group, jj):
    if kind in _PROJ_PART:
        base = (_PROJ_PART[kind] * N_GROUPS + group) * HEADS_PER_GROUP
    else:
        base = 0
    if kind.endswith("p"):
        row = lambda n: jnp.maximum(n - 1, 0)
    elif kind.endswith("n"):
        row = lambda n: jnp.minimum(n + 1, nb - 1)
    else:
        row = lambda n: n
    return pl.BlockSpec((r * SPAN, HEAD_DIM), lambda j, n: (row(n), base + j * hp + jj))


def _attn_merge_call(parts):
    S = parts[0].shape[1]

    def body(o1, s1, o2, s2, o3, s3, a_ref, ab_ref, lse_ref):
        for j in range(HEADS_PER_GROUP):
            sl = slice(j * HEAD_DIM, (j + 1) * HEAD_DIM)
            mx = jnp.maximum(jnp.maximum(s1[j], s2[j]), s3[j])
            w1 = jnp.exp(s1[j] - mx)
            w2 = jnp.exp(s2[j] - mx)
            w3 = jnp.exp(s3[j] - mx)
            den = w1 + w2 + w3
            a = (w1 * o1[j] + w2 * o2[j] + w3 * o3[j]) / den
            a_ref[:, sl] = a
            ab_ref[:, sl] = a.astype(BF16)
            lse_ref[:, sl] = mx + jnp.log(den)

    heads = pl.BlockSpec((HEADS_PER_GROUP, ROW_TILE, HEAD_DIM), lambda i: (0, i, 0))
    return pl.pallas_call(
        body, name="attn_merge", grid=(S // ROW_TILE,),
        in_specs=[heads] * 6, out_specs=[_row_spec(GROUP_WIDTH)] * 3,
        out_shape=[jax.ShapeDtypeStruct((S, GROUP_WIDTH), F32), jax.ShapeDtypeStruct((S, GROUP_WIDTH), BF16),
                   jax.ShapeDtypeStruct((S, GROUP_WIDTH), F32)],
        compiler_params=_params(("parallel",)),
    )(*parts)


def _attn_delta_call(a, da):
    S = a.shape[0]

    def body(a_ref, da_ref, d_ref):
        for j in range(HEADS_PER_GROUP):
            sl = slice(j * HEAD_DIM, (j + 1) * HEAD_DIM)
            d = jnp.sum(a_ref[:, sl] * da_ref[:, sl], axis=-1, keepdims=True)
            d_ref[:, sl] = jnp.broadcast_to(d, (ROW_TILE, HEAD_DIM))

    return pl.pallas_call(
        body, name="attn_delta", grid=(S // ROW_TILE,),
        in_specs=[_row_spec(GROUP_WIDTH)] * 2, out_specs=_row_spec(GROUP_WIDTH),
        out_shape=jax.ShapeDtypeStruct((S, GROUP_WIDTH), F32),
        compiler_params=_params(("parallel",)),
    )(a, da)


def _attn_bwd_call(proj, bias, da, lse, delta, group):
    S = proj.shape[0]
    r, nb, hp, per = _attn_plan(S, group)
    scale = HEAD_DIM ** -0.5
    kinds = ("q", "qn", "kp", "kc", "vp", "vc", "da", "dan", "lse", "lsen", "dl", "dln") if nb > 1 else ("q", "kc", "vc", "da", "lse", "dl")
    source = dict(da=da, dan=da, lse=lse, lsen=lse, dl=delta, dln=delta)

    def body(*refs):
        ins = {kind: refs[i * hp:(i + 1) * hp] for i, kind in enumerate(kinds)}
        b_ref, dq_ref, dk_ref, dv_ref, db_ref = refs[len(kinds) * hp:]
        n = pl.program_id(1)
        prev_ok, cur_ok, next_ok = _band_masks(n, nb)

        @pl.when(n == 0)
        def _():
            db_ref[...] = jnp.zeros_like(db_ref)

        band_ok = _wide_band_mask(n) if nb > 1 else cur_ok

        def residue(rho):
            rows = _residue_rows(rho, r)
            for j in range(hp):
                get = lambda kind: ins[kind][j][rows, :]
                q = get("q").astype(BF16)
                kc = get("kc").astype(BF16)
                vc = get("vc").astype(BF16)
                dav = get("da").astype(BF16)
                lse_q, dl_q = get("lse"), get("dl")
                if nb == 1:
                    pc = jnp.exp(jnp.where(cur_ok, _dot_nt(q, kc) * scale + b_ref[j, :, SPAN:], NEG_INF) - lse_q)
                    dsc = pc * (_dot_nt(dav, vc) - dl_q)
                    dsc_b = dsc.astype(BF16)
                    dq = _dot_nn(dsc_b, kc)
                    dk = _dot_tn(dsc_b, q)
                    dv = _dot_tn(pc.astype(BF16), dav)
                    db_ref[j, :, SPAN:] += dsc
                else:
                    qn = get("qn").astype(BF16)
                    dan = get("dan").astype(BF16)
                    keys = jnp.concatenate([get("kp").astype(BF16), kc], axis=0)
                    vals = jnp.concatenate([get("vp").astype(BF16), vc], axis=0)
                    wide = lambda t: jnp.concatenate([t, t], axis=1)
                    p = jnp.exp(jnp.where(band_ok, _dot_nt(q, keys) * scale + b_ref[j], NEG_INF) - wide(lse_q))
                    ds = p * (_dot_nt(dav, vals) - wide(dl_q))
                    dq = _dot_nn(ds.astype(BF16), keys)
                    db_ref[j] += ds
                    pn = jnp.exp(jnp.where(next_ok, _dot_nt(qn, kc) * scale + b_ref[j, :, :SPAN], NEG_INF) - get("lsen"))
                    dsn = pn * (_dot_nt(dan, vc) - get("dln"))
                    both = lambda cur_part, next_part: jnp.concatenate([cur_part.astype(BF16), next_part.astype(BF16)], axis=0)
                    dk = _dot_tn(both(ds[:, SPAN:], dsn), jnp.concatenate([q, qn], axis=0))
                    dv = _dot_tn(both(p[:, SPAN:], pn), jnp.concatenate([dav, dan], axis=0))
                dq_ref[j, rows, :] = dq * scale
                dk_ref[j, rows, :] = dk * scale
                dv_ref[j, rows, :] = dv

        _for_residues(r, per, residue)

    per_group = HEADS_PER_GROUP // hp
    band = (hp, SPAN, 2 * SPAN)
    in_specs = [_head_spec(r, nb, hp, kind, group, jj) for kind in kinds for jj in range(hp)]
    in_specs.append(pl.BlockSpec(band, lambda j, n: (group * per_group + j, 0, 0)))
    operands = [source.get(kind, proj) for kind in kinds for _ in range(hp)] + [bias]
    out = pl.BlockSpec((hp, r * SPAN, HEAD_DIM), lambda j, n: (j, n, 0))
    return pl.pallas_call(
        body, name=f"attn_bwd_g{group}", grid=(per_group, nb),
        in_specs=in_specs,
        out_specs=[out] * 3 + [pl.BlockSpec(band, lambda j, n: (j, 0, 0))],
        out_shape=[jax.ShapeDtypeStruct((HEADS_PER_GROUP, S, HEAD_DIM), F32)] * 3
        + [jax.ShapeDtypeStruct((HEADS_PER_GROUP, SPAN, 2 * SPAN), F32)],
        compiler_params=_params(("parallel", "arbitrary"), VMEM_LIMIT),
    )(*operands)


def _dproj_call(dqkv, tails):
    S = tails[0].shape[0]
    width = len(dqkv) * GROUP_WIDTH + sum(t.shape[1] for t in tails)

    def body(*refs):
        o_ref = refs[-1]
        col = 0
        for ref in refs[:len(dqkv)]:
            for j in range(HEADS_PER_GROUP):
                o_ref[:, col:col + HEAD_DIM] = ref[j].astype(BF16)
                col += HEAD_DIM
        for ref in refs[len(dqkv):-1]:
            o_ref[:, col:col + ref.shape[1]] = ref[...]
            col += ref.shape[1]

    heads = pl.BlockSpec((HEADS_PER_GROUP, ROW_TILE, HEAD_DIM), lambda i: (0, i, 0))
    return pl.pallas_call(
        body, name="dproj_assemble", grid=(S // ROW_TILE,),
        in_specs=[heads] * len(dqkv) + [_row_spec(t.shape[1]) for t in tails],
        out_specs=_row_spec(width), out_shape=jax.ShapeDtypeStruct((S, width), BF16),
        compiler_params=_params(("parallel",)),
    )(*dqkv, *tails)


def _tap_rows(xpad_ref, t0, k, width, pad):
    return xpad_ref[pl.ds(t0 + (pad - (width - 1 - k)), TIME_BLOCK), :]


def _conv_block(xpad_ref, t0, w_ref, width, pad):
    acc = None
    for k in range(width):
        term = w_ref[k:k + 1, :] * _tap_rows(xpad_ref, t0, k, width, pad)
        acc = term if acc is None else acc + term
    return acc


def _conv_transpose_block(dpad_ref, t0, w_ref, width):
    acc = None
    for k in range(width):
        term = w_ref[k:k + 1, :] * dpad_ref[pl.ds(t0 + (width - 1 - k), TIME_BLOCK), :]
        acc = term if acc is None else acc + term
    return acc


def _conv_weight_grad(xpad_ref, t0, dy, dw_ref, width, pad):
    for k in range(width):
        dw_ref[k:k + 1, :] += jnp.sum(dy * _tap_rows(xpad_ref, t0, k, width, pad), axis=0, keepdims=True)


def _time_loop(S, step):
    def it(tb, carry):
        step(pl.multiple_of(tb * TIME_BLOCK, TIME_BLOCK))
        return carry

    lax.fori_loop(0, S // TIME_BLOCK, it, 0)


def _conv_fwd_call(proj, col0, w, b):
    S = proj.shape[0]
    C = w.shape[1]
    nt = C // LANES
    v0, g0 = col0 // LANES, (col0 + C) // LANES

    def body(val_ref, gate_ref, w_ref, b_ref, o_ref, pad_ref):
        pad_ref[0:CONV_PAD, :] = jnp.zeros((CONV_PAD, LANES), F32)
        pad_ref[CONV_PAD:, :] = val_ref[...] * _sigmoid(gate_ref[...])

        def step(t0):
            o_ref[pl.ds(t0, TIME_BLOCK), :] = _conv_block(pad_ref, t0, w_ref, CONV_WIDTH, CONV_PAD) + b_ref[...]

        _time_loop(S, step)

    seq = lambda off: pl.BlockSpec((S, LANES), lambda i: (0, off + i))
    return pl.pallas_call(
        body, name="conv_module", grid=(nt,),
        in_specs=[seq(v0), seq(g0), pl.BlockSpec((CONV_WIDTH, LANES), lambda i: (0, i)), pl.BlockSpec((1, LANES), lambda i: (0, i))],
        out_specs=seq(0), out_shape=jax.ShapeDtypeStruct((S, C), F32),
        scratch_shapes=[pltpu.VMEM((S + CONV_PAD, LANES), F32)],
        compiler_params=_params(("parallel",)),
    )(proj, proj, w, b)


def _conv_bwd_call(proj, col0, w, dc1):
    S = proj.shape[0]
    C = w.shape[1]
    nt = C // LANES
    v0, g0 = col0 // LANES, (col0 + C) // LANES

    def body(val_ref, gate_ref, w_ref, dy_ref, dval_ref, dgate_ref, dw_ref, db_ref, xpad_ref, dpad_ref, dwacc_ref):
        xpad_ref[0:CONV_PAD, :] = jnp.zeros((CONV_PAD, LANES), F32)
        xpad_ref[CONV_PAD:, :] = val_ref[...] * _sigmoid(gate_ref[...])
        dpad_ref[0:S, :] = dy_ref[...]
        dpad_ref[S:, :] = jnp.zeros((CONV_PAD, LANES), F32)
        dwacc_ref[...] = jnp.zeros_like(dwacc_ref)

        def step(t0):
            rows = pl.ds(t0, TIME_BLOCK)
            _conv_weight_grad(xpad_ref, t0, dy_ref[rows, :], dwacc_ref, CONV_WIDTH, CONV_PAD)
            dc0 = _conv_transpose_block(dpad_ref, t0, w_ref, CONV_WIDTH)
            sg = _sigmoid(gate_ref[rows, :])
            dval_ref[rows, :] = (dc0 * sg).astype(BF16)
            dgate_ref[rows, :] = (dc0 * val_ref[rows, :] * sg * (1.0 - sg)).astype(BF16)

        _time_loop(S, step)
        dw_ref[...] = dwacc_ref[...]
        db_ref[...] = jnp.sum(dy_ref[...], axis=0, keepdims=True)

    seq = lambda off: pl.BlockSpec((S, LANES), lambda i: (0, off + i))
    return pl.pallas_call(
        body, name="conv_module_bwd", grid=(nt,),
        in_specs=[seq(v0), seq(g0), pl.BlockSpec((CONV_WIDTH, LANES), lambda i: (0, i)), seq(0)],
        out_specs=[seq(0), seq(0), pl.BlockSpec((CONV_PAD, LANES), lambda i: (0, i)), pl.BlockSpec((1, LANES), lambda i: (0, i))],
        out_shape=[jax.ShapeDtypeStruct((S, C), BF16), jax.ShapeDtypeStruct((S, C), BF16),
                   jax.ShapeDtypeStruct((CONV_PAD, C), F32), jax.ShapeDtypeStruct((1, C), F32)],
        scratch_shapes=[pltpu.VMEM((S + CONV_PAD, LANES), F32), pltpu.VMEM((S + CONV_PAD, LANES), F32),
                        pltpu.VMEM((CONV_PAD, LANES), F32)],
        compiler_params=_params(("parallel",)),
    )(proj, proj, w, dc1)


def _ffn_fwd_call(u, w, b):
    S, C2 = u.shape
    C = C2 // 2
    nt = C // LANES

    def body(ug_ref, uv_ref, wg_ref, wv_ref, bg_ref, bv_ref, f_ref, pg_ref, pv_ref):
        zeros = jnp.zeros((FFN_PAD, LANES), F32)
        pg_ref[0:FFN_PAD, :] = zeros
        pv_ref[0:FFN_PAD, :] = zeros
        pg_ref[FFN_PAD:, :] = ug_ref[...]
        pv_ref[FFN_PAD:, :] = uv_ref[...]

        def step(t0):
            cg = _conv_block(pg_ref, t0, wg_ref, FFN_CONV_WIDTH, FFN_PAD) + bg_ref[...]
            cv = _conv_block(pv_ref, t0, wv_ref, FFN_CONV_WIDTH, FFN_PAD) + bv_ref[...]
            f_ref[pl.ds(t0, TIME_BLOCK), :] = (_gelu(cg) * cv).astype(BF16)

        _time_loop(S, step)

    seq = lambda off: pl.BlockSpec((S, LANES), lambda i: (0, off + i))
    wsp = lambda off: pl.BlockSpec((FFN_CONV_WIDTH, LANES), lambda i: (0, off + i))
    bsp = lambda off: pl.BlockSpec((1, LANES), lambda i: (0, off + i))
    return pl.pallas_call(
        body, name="ffn_conv_geglu", grid=(nt,),
        in_specs=[seq(0), seq(nt), wsp(0), wsp(nt), bsp(0), bsp(nt)],
        out_specs=seq(0), out_shape=jax.ShapeDtypeStruct((S, C), BF16),
        scratch_shapes=[pltpu.VMEM((S + FFN_PAD, LANES), F32)] * 2,
        compiler_params=_params(("parallel",)),
    )(u, u, w, w, b, b)


def _ffn_bwd_call(u, w, b, df):
    S, C2 = u.shape
    C = C2 // 2
    nt = C // LANES

    def body(ug_ref, uv_ref, wg_ref, wv_ref, bg_ref, bv_ref, df_ref,
             du_ref, dwg_ref, dwv_ref, dbg_ref, dbv_ref,
             pg_ref, pv_ref, dg_ref, dv_ref, dwg_acc, dwv_acc, dbg_acc, dbv_acc):
        zeros = jnp.zeros((FFN_PAD, LANES), F32)
        pg_ref[0:FFN_PAD, :] = zeros
        pv_ref[0:FFN_PAD, :] = zeros
        pg_ref[FFN_PAD:, :] = ug_ref[...]
        pv_ref[FFN_PAD:, :] = uv_ref[...]
        dg_ref[S:, :] = zeros
        dv_ref[S:, :] = zeros
        dwg_acc[...] = jnp.zeros_like(dwg_acc)
        dwv_acc[...] = jnp.zeros_like(dwv_acc)
        dbg_acc[...] = jnp.zeros_like(dbg_acc)
        dbv_acc[...] = jnp.zeros_like(dbv_acc)

        def first(t0):
            rows = pl.ds(t0, TIME_BLOCK)
            cg = _conv_block(pg_ref, t0, wg_ref, FFN_CONV_WIDTH, FFN_PAD) + bg_ref[...]
            cv = _conv_block(pv_ref, t0, wv_ref, FFN_CONV_WIDTH, FFN_PAD) + bv_ref[...]
            dfb = df_ref[rows, :]
            gelu, gelu_grad = _gelu_and_grad(cg)
            dcg = dfb * cv * gelu_grad
            dcv = dfb * gelu
            dg_ref[rows, :] = dcg
            dv_ref[rows, :] = dcv
            _conv_weight_grad(pg_ref, t0, dcg, dwg_acc, FFN_CONV_WIDTH, FFN_PAD)
            _conv_weight_grad(pv_ref, t0, dcv, dwv_acc, FFN_CONV_WIDTH, FFN_PAD)
            dbg_acc[...] += jnp.sum(dcg, axis=0, keepdims=True)
            dbv_acc[...] += jnp.sum(dcv, axis=0, keepdims=True)

        def second(t0):
            rows = pl.ds(t0, TIME_BLOCK)
            du_ref[0, rows, :] = _conv_transpose_block(dg_ref, t0, wg_ref, FFN_CONV_WIDTH).astype(BF16)
            du_ref[1, rows, :] = _conv_transpose_block(dv_ref, t0, wv_ref, FFN_CONV_WIDTH).astype(BF16)

        _time_loop(S, first)
        _time_loop(S, second)
        dwg_ref[...] = dwg_acc[...]
        dwv_ref[...] = dwv_acc[...]
        dbg_ref[...] = dbg_acc[...]
        dbv_ref[...] = dbv_acc[...]

    seq = lambda off: pl.BlockSpec((S, LANES), lambda i: (0, off + i))
    wsp = lambda off: pl.BlockSpec((FFN_CONV_WIDTH, LANES), lambda i: (0, off + i))
    bsp = lambda off: pl.BlockSpec((1, LANES), lambda i: (0, off + i))
    return pl.pallas_call(
        body, name="ffn_conv_geglu_bwd", grid=(nt,),
        in_specs=[seq(0), seq(nt), wsp(0), wsp(nt), bsp(0), bsp(nt), seq(0)],
        out_specs=[pl.BlockSpec((2, S, LANES), lambda i: (0, 0, i)),
                   pl.BlockSpec((SUBLANES, LANES), lambda i: (0, i)), pl.BlockSpec((SUBLANES, LANES), lambda i: (0, i)),
                   bsp(0), bsp(0)],
        out_shape=[jax.ShapeDtypeStruct((2, S, C), BF16)] + [jax.ShapeDtypeStruct((SUBLANES, C), F32)] * 2
        + [jax.ShapeDtypeStruct((1, C), F32)] * 2,
        scratch_shapes=[pltpu.VMEM((S + FFN_PAD, LANES), F32)] * 4 + [pltpu.VMEM((SUBLANES, LANES), F32)] * 2
        + [pltpu.VMEM((1, LANES), F32)] * 2,
        compiler_params=_params(("parallel",)),
    )(u, u, w, w, b, b, df)


def _adamw(w_ref, g_ref, m_ref, v_ref, d_ref, mo_ref, vo_ref):
    gv = g_ref[...]
    mn = ADAM_B1 * m_ref[...] + (1.0 - ADAM_B1) * gv
    vn = ADAM_B2 * v_ref[...] + (1.0 - ADAM_B2) * (gv * gv)
    mo_ref[...] = mn
    vo_ref[...] = vn
    m_hat = mn * (1.0 / (1.0 - ADAM_B1 ** ADAM_STEP))
    v_hat = vn * (1.0 / (1.0 - ADAM_B2 ** ADAM_STEP))
    d_ref[...] = -ADAM_LR * (m_hat / (jnp.sqrt(v_hat) + ADAM_EPS) + ADAM_WD * w_ref[...])


def _adamw_call(w, g, m, v, name):
    R, C = w.shape
    tr = _row_tile(R, C)

    def body(w_ref, g_ref, m_ref, v_ref, go_ref, d_ref, mo_ref, vo_ref):
        go_ref[...] = g_ref[...]
        _adamw(w_ref, g_ref, m_ref, v_ref, d_ref, mo_ref, vo_ref)

    spec = pl.BlockSpec((tr, C), lambda i: (i, 0))
    return pl.pallas_call(
        body, name=name, grid=(R // tr,),
        in_specs=[spec] * 4, out_specs=[spec] * 4,
        out_shape=[jax.ShapeDtypeStruct((R, C), F32)] * 4,
        compiler_params=_params(("parallel",)),
    )(w, g, m, v)


def _adamw_small_call(ws, gs, ms, vs):
    n = len(ws)

    def body(*refs):
        w_refs, g_refs, m_refs, v_refs, d_refs, mo_refs, vo_refs = (refs[i * n:(i + 1) * n] for i in range(7))
        for i in range(n):
            _adamw(w_refs[i], g_refs[i], m_refs[i], v_refs[i], d_refs[i], mo_refs[i], vo_refs[i])

    whole = pl.BlockSpec(memory_space=pltpu.VMEM)
    outs = pl.pallas_call(
        body, name="adamw_small",
        in_specs=[whole] * (4 * n), out_specs=[whole] * (3 * n),
        out_shape=[jax.ShapeDtypeStruct(w.shape, F32) for w in ws] * 3,
    )(*ws, *gs, *ms, *vs)
    return outs[:n], outs[n:2 * n], outs[2 * n:]


def _position():
    return lax.axis_index("x"), lax.axis_index("y"), lax.axis_index("c")


def _chip_peers(x, y):
    return [(x, 1 - y), (1 - x, y), (1 - x, 1 - y)]


def _half_rows(ref, core, rows):
    h = rows // 2
    start = pl.multiple_of(core * h, 16)
    return ref.at[pl.ds(start, h), :] if len(ref.shape) == 2 else ref.at[:, pl.ds(start, h), :]


def _shard_half(ref, shard, core, rows):
    h = rows // 2
    return ref.at[shard, pl.ds(pl.multiple_of(core * h, 16), h), :]


ANY = pl.BlockSpec(memory_space=pl.ANY)


def _first_hop_copies(srcs, lands):
    x, y, c = _position()
    chip = 2 * x + y
    targets = [(px, py, c) for px, py in _chip_peers(x, y)] + [(x, y, 1 - c)]
    rows = srcs[0].shape[0]
    out = []
    for i, (s, l) in enumerate(zip(srcs, lands)):
        for k, dev in enumerate(targets):
            if i == 0 and k < 3:
                out.append((_half_rows(s, c, rows), _shard_half(l, chip, c, rows), dev, k))
            else:
                out.append((s, l.at[chip], dev, len(targets) * i + k))
    return out


def _second_hop_copies(srcs, lands):
    x, y, c = _position()
    rows = lands[0].shape[1]
    out = []
    for k, (px, py) in enumerate(_chip_peers(x, y)):
        half = _shard_half(lands[0], 2 * px + py, c, rows)
        out.append((half, half, (x, y, 1 - c), k))
    return out


HBM_SPEC = pl.BlockSpec(memory_space=pltpu.HBM)
SEM_SPEC = pl.BlockSpec(memory_space=pltpu.SEMAPHORE)
DATAFLOW = pltpu.SideEffectType.DATAFLOW_SIDE_EFFECTING


def _in_hbm(a):
    return pltpu.with_memory_space_constraint(a, pltpu.HBM)


def _split_start(name, groups, after, carry=None):
    spans, arrays = [], []
    for srcs, lands, _, _ in groups:
        spans.append((len(arrays), len(srcs), len(lands)))
        arrays += list(srcs) + list(lands)
    if carry is not None:
        arrays.append(carry)
    na, ng = len(arrays), len(groups)

    def body(*refs):
        sems, token = refs[na + 1:na + 1 + 2 * ng], refs[-1]
        for g, (_, _, _, copies) in enumerate(groups):
            off, ns, nl = spans[g]
            for src, dst, dev, idx in copies(refs[off:off + ns], refs[off + ns:off + ns + nl]):
                pltpu.make_async_remote_copy(src_ref=src, dst_ref=dst, send_sem=sems[2 * g].at[idx], recv_sem=sems[2 * g + 1].at[idx],
                                             device_id=dev, device_id_type=MESH).start()
        token[...] = jnp.zeros_like(token)

    outs = pl.pallas_call(
        body, name=name,
        in_specs=[HBM_SPEC] * na + [ANY],
        out_specs=[SEM_SPEC] * (2 * ng) + [HBM_SPEC] * na + [pl.BlockSpec(memory_space=pltpu.VMEM)],
        out_shape=[pltpu.SemaphoreType.DMA((n_sems,)) for _, _, n_sems, _ in groups for _ in range(2)]
        + [pltpu.HBM(a.shape, a.dtype) for a in arrays] + [jax.ShapeDtypeStruct((SUBLANES, LANES), F32)],
        input_output_aliases={i: 2 * ng + i for i in range(na)},
        compiler_params=pltpu.CompilerParams(has_side_effects=DATAFLOW),
    )(*[_in_hbm(a) for a in arrays], after)
    started = []
    for g, (off, ns, nl) in enumerate(spans):
        thru = outs[2 * ng + off:2 * ng + off + ns + nl]
        started.append(dict(send=outs[2 * g], recv=outs[2 * g + 1], srcs=list(thru[:ns]), lands=list(thru[ns:]),
                            tile=outs[-1], token=outs[-1][0, 0], carry=None if carry is None else outs[2 * ng + na - 1]))
    return started


def _split_wait(name, started, copies, after):
    n, m = len(started["srcs"]), len(started["lands"])

    def body(*refs):
        src_refs, land_refs = refs[:n], refs[n:n + m]
        send_sem, recv_sem = refs[n + m], refs[n + m + 1]
        for src, dst, dev, idx in copies(src_refs, land_refs):
            cp = pltpu.make_async_remote_copy(src_ref=src, dst_ref=dst, send_sem=send_sem.at[idx], recv_sem=recv_sem.at[idx],
                                              device_id=dev, device_id_type=MESH)
            cp.wait_send()
            cp.wait_recv()

    arrays = started["srcs"] + started["lands"]
    outs = pl.pallas_call(
        body, name=name,
        in_specs=[HBM_SPEC] * (n + m) + [SEM_SPEC, SEM_SPEC, ANY],
        out_specs=[HBM_SPEC] * (n + m),
        out_shape=[pltpu.HBM(a.shape, a.dtype) for a in arrays],
        input_output_aliases={i: i for i in range(n + m)},
        compiler_params=pltpu.CompilerParams(has_side_effects=DATAFLOW),
    )(*arrays, started["send"], started["recv"], after)
    return list(outs)


def _gather_copies(srcs, lands):
    x, y, c = _position()
    chip = 2 * x + y
    targets = [(px, py, c) for px, py in _chip_peers(x, y)] + [(x, y, 1 - c)]
    return [(s, l.at[chip], dev, len(targets) * i + k) for i, (s, l) in enumerate(zip(srcs, lands)) for k, dev in enumerate(targets)]


def _sibling_copies(srcs, lands):
    x, y, c = _position()
    return [(_half_rows(srcs[0], 1 - c, srcs[0].shape[1]), lands[0], (x, y, 1 - c), 0)]


def _exchange_copies(srcs, lands):
    x, y, c = _position()
    return [(srcs[0].at[2 * px + py], lands[0].at[k], (px, py, c), k) for k, (px, py) in enumerate(_chip_peers(x, y))]


def _pair_sum_call(grad, recv, core, name):
    _, h, B = recv.shape
    tr = _row_tile(h, B)

    def body(core_ref, g_ref, r_ref, o_ref, ob_ref):
        s = g_ref[...] + r_ref[...]
        o_ref[...] = s
        ob_ref[...] = s.astype(BF16)

    g_spec = pl.BlockSpec((None, tr, B), lambda q, i, core_ref: (q, core_ref[0] * (h // tr) + i, 0))
    spec = pl.BlockSpec((None, tr, B), lambda q, i, core_ref: (q, i, 0))
    return pl.pallas_call(
        body, name=name,
        grid_spec=pltpu.PrefetchScalarGridSpec(num_scalar_prefetch=1, grid=(N_CHIPS, h // tr), in_specs=[g_spec, spec],
                                               out_specs=[spec, spec]),
        out_shape=[jax.ShapeDtypeStruct(recv.shape, F32), jax.ShapeDtypeStruct(recv.shape, BF16)],
        compiler_params=_params(("parallel", "parallel")),
    )(core, grad, recv)


def _chip_sum_call(partial, recv, chip_core, name):
    _, h, B = recv.shape
    tr = _row_tile(h, B)

    def body(cc_ref, p_ref, r_ref, o_ref):
        o_ref[...] = ((p_ref[...] + r_ref[0].astype(F32)) + r_ref[1].astype(F32)) + r_ref[2].astype(F32)

    return pl.pallas_call(
        body, name=name,
        grid_spec=pltpu.PrefetchScalarGridSpec(
            num_scalar_prefetch=1, grid=(h // tr,),
            in_specs=[pl.BlockSpec((None, tr, B), lambda i, cc_ref: (cc_ref[0], i, 0)),
                      pl.BlockSpec((3, tr, B), lambda i, cc_ref: (0, i, 0))],
            out_specs=pl.BlockSpec((tr, B), lambda i, cc_ref: (cc_ref[1] * (h // tr) + i, 0))),
        out_shape=jax.ShapeDtypeStruct((2 * h, B), F32),
        compiler_params=_params(("parallel",)),
    )(chip_core, partial, recv)


def _sibling_assemble_call(shards, name="grad_sibling_assemble"):
    n = len(shards)

    def body(*refs):
        ins, outs = refs[:n], refs[n:2 * n]
        send_sems, recv_sems = refs[2 * n:]
        x, y, c = _position()
        copies = []
        for i in range(n):
            rows = shards[i].shape[0]
            cp = pltpu.make_async_remote_copy(src_ref=_half_rows(ins[i], c, rows), dst_ref=_half_rows(outs[i], c, rows),
                                              send_sem=send_sems.at[i], recv_sem=recv_sems.at[i],
                                              device_id=(x, y, 1 - c), device_id_type=MESH)
            cp.start()
            copies.append(cp)
        for cp in copies:
            cp.wait()

    return pl.pallas_call(
        body, name=name,
        in_specs=[ANY] * n, out_specs=[ANY] * n,
        out_shape=[jax.ShapeDtypeStruct(s.shape, F32) for s in shards],
        input_output_aliases={i: i for i in range(n)},
        scratch_shapes=[pltpu.SemaphoreType.DMA((n,)), pltpu.SemaphoreType.DMA((n,))],
    )(*shards)


N_DEVICES = 8


def _allsum_copies(srcs, lands):
    x, y, c = _position()
    me = 4 * x + 2 * y + c
    out = []
    for k in range(1, N_DEVICES):
        peer = (1 - x if k & 4 else x, 1 - y if k & 2 else y, 1 - c if k & 1 else c)
        out.append((srcs[0], lands[0].at[me], peer, k - 1))
    return out


def _ordered_sum_call(mine, landed, me):
    rows = mine.shape[0]

    def body(me_ref, x_ref, l_ref, o_ref):
        acc = jnp.where(me_ref[0] == 0, x_ref[...], l_ref[0])
        for d in range(1, N_DEVICES):
            acc = acc + jnp.where(me_ref[0] == d, x_ref[...], l_ref[d])
        o_ref[...] = acc

    return pl.pallas_call(
        body, name="small_grad_sum",
        in_specs=[pl.BlockSpec(memory_space=pltpu.SMEM), pl.BlockSpec(memory_space=pltpu.VMEM), pl.BlockSpec(memory_space=pltpu.VMEM)],
        out_specs=pl.BlockSpec(memory_space=pltpu.VMEM),
        out_shape=jax.ShapeDtypeStruct((rows, LANES), F32),
    )(me, mine, landed)


def _pack(arrays):
    flat = jnp.concatenate([a.reshape(-1).astype(F32) for a in arrays])
    rows = -(-flat.shape[0] // LANES)
    rows = -(-rows // SUBLANES) * SUBLANES
    flat = jnp.pad(flat, (0, rows * LANES - flat.shape[0]))
    return flat.reshape(rows, LANES)


def _unpack(packed, shapes):
    flat = packed.reshape(-1)
    out, off = [], 0
    for shp in shapes:
        size = int(np.prod(shp))
        out.append(flat[off:off + size].reshape(shp))
        off += size
    return out


def _local_step(xs, target, P, late_weights, on_grad):
    S, D = xs.shape
    qkv_width = 3 * N_HEADS * HEAD_DIM
    glu_col0, gate_col0 = qkv_width, qkv_width + 2 * D
    shard_major = lambda g: g.reshape(N_CHIPS, g.shape[0] // N_CHIPS, g.shape[1])

    h1 = _rms_fwd_call(xs, P["norm_mix_pre"])
    buckets = _bucket_tables()
    bias = _bias_table_call(P["rel_bias"] + 0.0 * h1[0, 0].astype(F32), buckets)
    P = dict(P, **late_weights("in", bias))
    proj = _matmul(h1, P["w_in"], "nn", "proj_in")
    parts = []
    for g in range(N_GROUPS):
        parts += _attn_fwd_call(proj, bias, g)
    a, a_bf, lse = _attn_merge_call(parts)
    P = dict(P, **late_weights("mix", a_bf))
    y_a = _matmul(a_bf, P["w_attn_out"], "nn", "attn_out")
    c1 = _conv_fwd_call(proj, glu_col0, P["conv_dw_w"], P["conv_dw_b"])
    cact = _ln_silu_call(c1, P["conv_ln_g"], P["conv_ln_b"])
    y_c = _matmul(cact, P["conv_pw_w"], "nn", "conv_pw")
    mixed = _mix_call(proj, gate_col0, P["b_gate"], y_a, y_c)
    out = _matmul(mixed, P["w_out"], "nn", "mix_out")
    x1, h2 = _res1_call(xs, out, P["norm_mix_post"], P["norm_ffn_pre"])
    P = dict(P, **late_weights("up", h2))
    u = _matmul(h2, P["w_up"], "nn", "ffn_up")
    f = _ffn_fwd_call(u, P["ffn_conv_w"], P["ffn_conv_b"])
    P = dict(P, **late_weights("down", f))
    yff = _matmul(f, P["w_down"], "nn", "ffn_down")
    loss_tile, dx2, dyff, dg_ffn_post = _loss_call(yff, x1, P["norm_ffn_post"], target)

    G = {}
    G["norm_ffn_post"] = dg_ffn_post
    zero = on_grad("w_down", shard_major(_matmul(f, dyff, "tn", "ffn_down_dw")))
    df = _matmul(dyff, P["w_down"], "nt", "ffn_down_dx")
    du, dwg, dwv, dbg, dbv = _ffn_bwd_call(u, P["ffn_conv_w"], P["ffn_conv_b"] + zero, df)
    G["ffn_conv_w"] = jnp.concatenate([dwg[:FFN_CONV_WIDTH], dwv[:FFN_CONV_WIDTH]], axis=1)
    G["ffn_conv_b"] = jnp.concatenate([dbg, dbv], axis=1)
    zero = on_grad("w_up", _matmul(h2, du, "tn", "ffn_up_dw", out_shards=True))
    dh2 = _matmul(du, P["w_up"], "nt", "ffn_up_dx")
    dx1, dout, G["norm_ffn_pre"], G["norm_mix_post"] = _mid_bwd_call(x1, P["norm_ffn_pre"] + zero, dh2, dx2, out, P["norm_mix_post"])
    zero = on_grad("w_out", shard_major(_matmul(mixed, dout, "tn", "mix_out_dw")))
    dmixed = _matmul(dout, P["w_out"], "nt", "mix_out_dx")
    dya, dyc, dga, dgc, dba, dbc = _mix_bwd_call(dmixed, proj, gate_col0, P["b_gate"] + zero, y_a, y_c)
    G["b_gate"] = jnp.concatenate([dba, dbc], axis=1)
    zero = on_grad("w_attn_out", _matmul(a_bf, dya, "tn", "attn_out_dw", out_shards=True))
    zero = zero + on_grad("conv_pw_w", shard_major(_matmul(cact, dyc, "tn", "conv_pw_dw")))
    da = _matmul(dya, P["w_attn_out"], "nt", "attn_out_dx")
    dcact = _matmul(dyc, P["conv_pw_w"], "nt", "conv_pw_dx")
    dc1, G["conv_ln_g"], G["conv_ln_b"] = _ln_silu_bwd_call(c1, P["conv_ln_g"] + zero, P["conv_ln_b"], dcact)
    dval, dgate, dw_dw, G["conv_dw_b"] = _conv_bwd_call(proj, glu_col0, P["conv_dw_w"], dc1)
    G["conv_dw_w"] = dw_dw[:CONV_WIDTH]
    delta = _attn_delta_call(a, da)
    dqs, dks, dvs, dbs = [], [], [], []
    for g in range(N_GROUPS):
        dq, dk, dv, db = _attn_bwd_call(proj, bias, da, lse, delta, g)
        dqs.append(dq)
        dks.append(dk)
        dvs.append(dv)
        dbs.append(db)
    G["rel_bias"] = _bias_grad_call(jnp.concatenate(dbs, axis=0), buckets)
    dproj = _dproj_call(dqs + dks + dvs, [dval, dgate, dga, dgc])
    zero, dproj = on_grad("w_in", _matmul(h1, dproj, "tn", "proj_in_dw", out_shards=True), carry=dproj)
    dh1 = _matmul(dproj, P["w_in"], "nt", "proj_in_dx")
    zero = zero + on_grad(None, dh1)
    grad_x, G["norm_mix_pre"] = _in_bwd_call(xs, P["norm_mix_pre"] + zero, dh1, dx1)
    return loss_tile, grad_x, G


def kernel(x, w_in, b_gate, rel_bias, w_attn_out, conv_dw_w, conv_dw_b, conv_ln_g, conv_ln_b, conv_pw_w, w_out, norm_mix_pre, norm_mix_post, norm_ffn_pre, norm_ffn_post, w_up, ffn_conv_w, ffn_conv_b, w_down, loss_target, m_w_in, m_b_gate, m_rel_bias, m_w_attn_out, m_conv_dw_w, m_conv_dw_b, m_conv_ln_g, m_conv_ln_b, m_conv_pw_w, m_w_out, m_norm_mix_pre, m_norm_mix_post, m_norm_ffn_pre, m_norm_ffn_post, m_w_up, m_ffn_conv_w, m_ffn_conv_b, m_w_down, v_w_in, v_b_gate, v_rel_bias, v_w_attn_out, v_conv_dw_w, v_conv_dw_b, v_conv_ln_g, v_conv_ln_b, v_conv_pw_w, v_w_out, v_norm_mix_pre, v_norm_mix_post, v_norm_ffn_pre, v_norm_ffn_post, v_w_up, v_ffn_conv_w, v_ffn_conv_b, v_w_down):
    weights = dict(w_in=w_in, b_gate=b_gate, rel_bias=rel_bias, w_attn_out=w_attn_out, conv_dw_w=conv_dw_w, conv_dw_b=conv_dw_b,
                   conv_ln_g=conv_ln_g, conv_ln_b=conv_ln_b, conv_pw_w=conv_pw_w, w_out=w_out, norm_mix_pre=norm_mix_pre,
                   norm_mix_post=norm_mix_post, norm_ffn_pre=norm_ffn_pre, norm_ffn_post=norm_ffn_post, w_up=w_up,
                   ffn_conv_w=ffn_conv_w, ffn_conv_b=ffn_conv_b, w_down=w_down)
    m_in = dict(w_in=m_w_in, b_gate=m_b_gate, rel_bias=m_rel_bias, w_attn_out=m_w_attn_out, conv_dw_w=m_conv_dw_w,
                conv_dw_b=m_conv_dw_b, conv_ln_g=m_conv_ln_g, conv_ln_b=m_conv_ln_b, conv_pw_w=m_conv_pw_w, w_out=m_w_out,
                norm_mix_pre=m_norm_mix_pre, norm_mix_post=m_norm_mix_post, norm_ffn_pre=m_norm_ffn_pre,
                norm_ffn_post=m_norm_ffn_post, w_up=m_w_up, ffn_conv_w=m_ffn_conv_w, ffn_conv_b=m_ffn_conv_b, w_down=m_w_down)
    v_in = dict(w_in=v_w_in, b_gate=v_b_gate, rel_bias=v_rel_bias, w_attn_out=v_w_attn_out, conv_dw_w=v_conv_dw_w,
                conv_dw_b=v_conv_dw_b, conv_ln_g=v_conv_ln_g, conv_ln_b=v_conv_ln_b, conv_pw_w=v_conv_pw_w, w_out=v_w_out,
                norm_mix_pre=v_norm_mix_pre, norm_mix_post=v_norm_mix_post, norm_ffn_pre=v_norm_ffn_pre,
                norm_ffn_post=v_norm_ffn_post, w_up=v_w_up, ffn_conv_w=v_ffn_conv_w, ffn_conv_b=v_ffn_conv_b, w_down=v_w_down)
    names = list(weights)
    xi, yi, ci = _position()
    chip = 2 * xi + yi
    core_arr = jnp.reshape(ci, (1,)).astype(jnp.int32)

    xs = x[0]
    target = loss_target[0]
    S, D = xs.shape

    big = ["w_in", "w_attn_out", "conv_pw_w", "w_out", "w_up", "w_down"]
    row_sharded = ("conv_pw_w", "w_out", "w_down")
    bf16_shard = {k: weights[k][0].astype(BF16) for k in big}
    natural = lambda k, g: g.reshape(-1, g.shape[2]) if k in row_sharded else g
    first_srcs = [bf16_shard["w_in"], conv_dw_w[0], ffn_conv_w[0]]
    first_lands = [lax.empty((N_CHIPS,) + s.shape, s.dtype) for s in first_srcs]
    (first_hop,) = _split_start("gather_in_start", [(first_srcs, first_lands, 4 * len(first_srcs), _first_hop_copies)], core_arr)
    launched = first_hop["token"]
    late_sets = dict(mix=["w_attn_out", "conv_pw_w", "w_out"], up=["w_up"], down=["w_down"])
    late_groups = []
    for keys in late_sets.values():
        srcs = [bf16_shard[k] for k in keys]
        late_groups.append((srcs, [lax.empty((N_CHIPS,) + s.shape, BF16) for s in srcs], 4 * len(keys), _gather_copies))
    started = {}

    def late_weights(tag, after):
        if tag == "in":
            w_in_halves, dw4, fc4 = _split_wait("gather_in_wait", first_hop, _first_hop_copies, after)[len(first_srcs):]
            (second_hop,) = _split_start("gather_in_pass_start", [([], [w_in_halves], 3, _second_hop_copies)], dw4)
            (w_in_full,) = _split_wait("gather_in_pass_wait", second_hop, _second_hop_copies, second_hop["tile"])
            started.update(zip(late_sets, _split_start("gather_late_start", late_groups, dw4, carry=w_in_full)))
            return dict(w_in=started["mix"]["carry"], conv_dw_w=jnp.concatenate(list(dw4), axis=1), ffn_conv_w=jnp.concatenate(list(fc4), axis=1))
        landed = _split_wait(f"gather_{tag}_wait", started[tag], _gather_copies, after)[len(late_sets[tag]):]
        return {k: natural(k, g) for k, g in zip(late_sets[tag], landed)}

    chip_core = jnp.stack([chip, ci]).astype(jnp.int32)
    exchanging, pending = {}, {}

    def launch(tag, g3, after, carry=None):
        keys, groups, partial = [], [], {}
        for k in list(exchanging):
            gk, r1 = _split_wait(f"sibling_exchange_wait_{k}", exchanging.pop(k), _sibling_copies, after)
            partial[k], s16 = _pair_sum_call(gk, r1, core_arr, f"pair_sum_{k}")
            keys.append(k)
            groups.append(([s16], [lax.empty((3,) + s16.shape[1:], BF16)], 3, _exchange_copies))
        if g3 is not None:
            groups.append(([g3], [lax.empty((N_CHIPS, g3.shape[1] // 2, g3.shape[2]), F32)], 1, _sibling_copies))
        begun = _split_start(f"grad_exchange_start_{tag}", groups, core_arr, carry)
        for k, st in zip(keys, begun):
            pending[k] = (partial[k], st)
        if g3 is not None:
            exchanging[tag] = begun[-1]
        return begun[0]["token"] if carry is None else (begun[0]["token"], begun[0]["carry"])

    def on_grad(k, g3, carry=None):
        if k is None:
            return launch("last", None, g3[:SUBLANES, :LANES])
        return launch(k, g3, g3[0, :SUBLANES, :LANES], carry)

    def finish(keys, after, tag):
        halves = []
        for k in keys:
            s32, st = pending[k]
            recv2 = _split_wait(f"chip_exchange_wait_{k}", st, _exchange_copies, after)[1]
            halves.append(_chip_sum_call(s32, recv2, chip_core, f"chip_sum_{k}"))
        return dict(zip(keys, _sibling_assemble_call(halves, f"grad_sibling_assemble_{tag}")))

    P = dict(b_gate=b_gate, rel_bias=rel_bias, conv_dw_b=conv_dw_b, conv_ln_g=conv_ln_g, conv_ln_b=conv_ln_b,
             norm_mix_pre=norm_mix_pre + launched, norm_mix_post=norm_mix_post, norm_ffn_pre=norm_ffn_pre,
             norm_ffn_post=norm_ffn_post, ffn_conv_b=ffn_conv_b)
    loss_tile, grad_x, G = _local_step(xs, target, P, late_weights, on_grad)

    small = [k for k in names if k not in big]
    packed = _pack([loss_tile[:1]] + [G[k] for k in small])
    (allsum,) = _split_start("small_grad_allsum_start",
                             [([packed], [jnp.zeros((N_DEVICES,) + packed.shape, F32)], N_DEVICES - 1, _allsum_copies)], core_arr)

    reduced, grads, deltas, new_m, new_v = {}, {}, {}, {}, {}

    def update(keys):
        for k in keys:
            gk, d, mn, vn = _adamw_call(weights[k][0], reduced[k], m_in[k][0], v_in[k][0], f"adamw_{k}")
            grads[k], deltas[k], new_m[k], new_v[k] = gk[None], d[None], mn[None], vn[None]

    others = [k for k in big if k != "w_in"]
    reduced.update(finish(others, allsum["tile"], "others"))
    update(others)
    reduced.update(finish(["w_in"], deltas["w_up"], "w_in"))
    update(["w_in"])

    me = jnp.reshape(4 * xi + 2 * yi + ci, (1,)).astype(jnp.int32)
    mine, landed = _split_wait("small_grad_allsum_wait", allsum, _allsum_copies, deltas["w_in"])
    summed_block = _ordered_sum_call(mine, landed, me)
    loss_row, *summed = _unpack(summed_block, [(1, LANES)] + [G[k].shape for k in small])
    loss = loss_row[0, 0]
    for k, gsum in zip(small, summed):
        if k in ("conv_dw_w", "ffn_conv_w"):
            cols = weights[k].shape[2]
            reduced[k] = lax.dynamic_slice_in_dim(gsum, chip * cols, cols, axis=1)
        else:
            reduced[k] = gsum
    for k in small:
        grads[k] = reduced[k].reshape(weights[k].shape)
    ds, mns, vns = _adamw_small_call([weights[k] for k in small], [grads[k] for k in small],
                                     [m_in[k] for k in small], [v_in[k] for k in small])
    deltas.update(zip(small, ds))
    new_m.update(zip(small, mns))
    new_v.update(zip(small, vns))

    return (loss, grad_x[None], *[grads[k] for k in names], *[deltas[k] for k in names],
            *[new_m[k] for k in names], *[new_v[k] for k in names])
```

```python
import functools
import math

import jax
import jax.numpy as jnp
import numpy as np
from jax import lax
from jax.experimental import pallas as pl
from jax.experimental.pallas import tpu as pltpu

F32 = jnp.float32
BF16 = jnp.bfloat16
MESH = pl.DeviceIdType.MESH

HEAD_DIM = 128
HEADS_PER_GROUP = 4
DILATED_PATTERNS = ((128, 1), (512, 4), (2048, 16))
N_GROUPS = 3
N_HEADS = N_GROUPS * HEADS_PER_GROUP
SPAN = 128
GROUP_WIDTH = HEADS_PER_GROUP * HEAD_DIM
CONV_WIDTH = 31
FFN_CONV_WIDTH = 3
N_BUCKETS = 32
MAX_DISTANCE = 2048
RMS_EPS = 1e-6
LN_EPS = 1e-5
NEG_INF = -1e30
ADAM_LR = 0.001
ADAM_B1 = 0.9
ADAM_B2 = 0.999
ADAM_EPS = 1e-08
ADAM_WD = 0.01
ADAM_STEP = 10

LANES = 128
SUBLANES = 8
ROW_TILE = 512
GATE_ROWS, GATE_COLS = 512, 512
TIME_BLOCK = 128
CONV_PAD = 32
FFN_PAD = 8
VMEM_LIMIT = 56 << 20


def _params(sem=None, vmem=None):
    kw = {}
    if sem is not None:
        kw["dimension_semantics"] = sem
    if vmem is not None:
        kw["vmem_limit_bytes"] = vmem
    return pltpu.CompilerParams(**kw)


def _pick(n, cands):
    for c in cands:
        if n % c == 0:
            return c
    return n


ELEMENTWISE_TILE_BYTES = 3 << 19


def _row_tile(rows, cols):
    for align in (16, SUBLANES):
        fits = [t for t in range(align, rows + 1, align) if rows % t == 0 and t * cols * 4 <= ELEMENTWISE_TILE_BYTES]
        if fits:
            return max(fits)
    return SUBLANES


N_CHIPS = 4
M_TILES = (1024, 1408, 512, 256, 128)
N_TILES = (1024, 512, 1408, 256, 128)
K_TILES = (2176, 2048, 1408, 1024, 512, 256, 128)


def _matmul(a, b, mode, name, out_shards=False, tm=None):
    assert a.dtype == BF16 and b.dtype == BF16, (name, a.dtype, b.dtype)
    b3 = b.ndim == 3
    tn = tk = None
    halves = None
    if mode == "nn":
        M, K = a.shape
        N = b.shape[-1] * (N_CHIPS if b3 else 1)
        tn = b.shape[-1] if b3 else None
    elif mode == "nt":
        if a.ndim == 3:
            halves = a.shape[2]
        M, K = a.shape[-2], a.shape[-1] * (a.shape[0] if a.ndim == 3 else 1)
        N = b.shape[-2]
        tk = b.shape[-1] if b3 else None
    else:
        if b3:
            halves = b.shape[2]
        K, M = a.shape
        N = b.shape[-1] * (b.shape[0] if b3 else 1)
        tn = N // N_CHIPS if out_shards else None
    tm = tm or _pick(M, M_TILES)
    tn = tn or _pick(N, N_TILES)
    tk = tk or _pick(K, K_TILES)
    nk = K // tk
    dn = {"nn": (((1,), (0,)), ((), ())), "nt": (((1,), (1,)), ((), ())), "tn": (((0,), (0,)), ((), ()))}[mode]

    def body(a_ref, b_ref, o_ref):
        if nk == 1:
            o_ref[...] = lax.dot_general(a_ref[...], b_ref[...], dn, preferred_element_type=F32)
        else:
            @pl.when(pl.program_id(2) == 0)
            def _():
                o_ref[...] = jnp.zeros_like(o_ref)

            o_ref[...] += lax.dot_general(a_ref[...], b_ref[...], dn, preferred_element_type=F32)

    if mode == "tn":
        a_spec = pl.BlockSpec((tk, tm), lambda i, j, k: (k, i))
    elif halves:
        per = halves // tk
        a_spec = pl.BlockSpec((None, tm, tk), lambda i, j, k: (k // per, i, k % per))
    else:
        a_spec = pl.BlockSpec((tm, tk), lambda i, j, k: (i, k))
    if mode == "nn":
        b_spec = pl.BlockSpec((None, tk, tn), lambda i, j, k: (j, k, 0)) if b3 else pl.BlockSpec((tk, tn), lambda i, j, k: (k, j))
    elif mode == "nt":
        b_spec = pl.BlockSpec((None, tn, tk), lambda i, j, k: (k, j, 0)) if b3 else pl.BlockSpec((tn, tk), lambda i, j, k: (j, k))
    elif halves:
        per = halves // tn
        b_spec = pl.BlockSpec((None, tk, tn), lambda i, j, k: (j // per, k, j % per))
    else:
        b_spec = pl.BlockSpec((tk, tn), lambda i, j, k: (k, j))
    if out_shards:
        out_spec = pl.BlockSpec((None, tm, tn), lambda i, j, k: (j, i, 0))
        out_shape = jax.ShapeDtypeStruct((N_CHIPS, M, tn), F32)
    else:
        out_spec = pl.BlockSpec((tm, tn), lambda i, j, k: (i, j))
        out_shape = jax.ShapeDtypeStruct((M, N), F32)
    return pl.pallas_call(
        body, name=name, grid=(M // tm, N // tn, nk),
        in_specs=[a_spec, b_spec], out_specs=out_spec, out_shape=out_shape,
        compiler_params=_params(("parallel", "parallel", "arbitrary"), VMEM_LIMIT),
    )(a, b)


def _rms(x, g):
    r = lax.rsqrt(jnp.mean(x * x, axis=-1, keepdims=True) + RMS_EPS)
    return x * r * g


def _rms_bwd(x, g, dy):
    r = lax.rsqrt(jnp.mean(x * x, axis=-1, keepdims=True) + RMS_EPS)
    n = x * r
    dn = dy * g
    dx = r * (dn - n * jnp.mean(dn * n, axis=-1, keepdims=True))
    return dx, jnp.sum(dy * n, axis=0, keepdims=True)


def _sigmoid(x):
    return 1.0 / (1.0 + jnp.exp(-x))


_GELU_C = math.sqrt(2.0 / math.pi)


def _gelu(x):
    return 0.5 * x * (1.0 + jnp.tanh(_GELU_C * (x + 0.044715 * x * x * x)))


def _gelu_and_grad(x):
    x2 = x * x
    t = jnp.tanh(_GELU_C * x * (1.0 + 0.044715 * x2))
    half = 0.5 * (1.0 + t)
    return x * half, half + (0.5 * _GELU_C) * x * (1.0 - t * t) * (1.0 + (3.0 * 0.044715) * x2)


def _row_spec(width, col_block=0):
    return pl.BlockSpec((ROW_TILE, width), lambda i: (i, col_block))


def _vec_spec(width, col_block=0):
    return pl.BlockSpec((1, width), lambda i: (0, col_block))


def _accumulate(ref, part):
    @pl.when(pl.program_id(0) == 0)
    def _():
        ref[...] = part

    @pl.when(pl.program_id(0) > 0)
    def _():
        ref[...] += part


def _rms_fwd_call(x, g):
    S, D = x.shape

    def body(x_ref, g_ref, h_ref):
        h_ref[...] = _rms(x_ref[...], g_ref[...]).astype(BF16)

    return pl.pallas_call(
        body, name="rms_mix_pre", grid=(S // ROW_TILE,),
        in_specs=[_row_spec(D), _vec_spec(D)], out_specs=_row_spec(D),
        out_shape=jax.ShapeDtypeStruct((S, D), BF16),
        compiler_params=_params(("parallel",)),
    )(x, g)


def _ln_silu_call(c1, g, b):
    S, C = c1.shape

    def body(c_ref, g_ref, b_ref, o_ref):
        xv = c_ref[...]
        mu = jnp.mean(xv, axis=-1, keepdims=True)
        xc = xv - mu
        var = jnp.mean(xc * xc, axis=-1, keepdims=True)
        z = xc * lax.rsqrt(var + LN_EPS) * g_ref[...] + b_ref[...]
        o_ref[...] = (z * _sigmoid(z)).astype(BF16)

    return pl.pallas_call(
        body, name="conv_ln_silu", grid=(S // ROW_TILE,),
        in_specs=[_row_spec(C), _vec_spec(C), _vec_spec(C)], out_specs=_row_spec(C),
        out_shape=jax.ShapeDtypeStruct((S, C), BF16),
        compiler_params=_params(("parallel",)),
    )(c1, g, b)


def _ln_silu_bwd_call(c1, g, b, dc):
    S, C = c1.shape

    def body(c_ref, g_ref, b_ref, dc_ref, dx_ref, dg_ref, db_ref):
        xv = c_ref[...]
        mu = jnp.mean(xv, axis=-1, keepdims=True)
        xc = xv - mu
        rs = lax.rsqrt(jnp.mean(xc * xc, axis=-1, keepdims=True) + LN_EPS)
        xh = xc * rs
        z = xh * g_ref[...] + b_ref[...]
        sg = _sigmoid(z)
        dz = dc_ref[...] * (sg * (1.0 + z * (1.0 - sg)))
        dxh = dz * g_ref[...]
        dx_ref[...] = rs * (dxh - jnp.mean(dxh, axis=-1, keepdims=True) - xh * jnp.mean(dxh * xh, axis=-1, keepdims=True))
        _accumulate(dg_ref, jnp.sum(dz * xh, axis=0, keepdims=True))
        _accumulate(db_ref, jnp.sum(dz, axis=0, keepdims=True))

    return pl.pallas_call(
        body, name="conv_ln_silu_bwd", grid=(S // ROW_TILE,),
        in_specs=[_row_spec(C), _vec_spec(C), _vec_spec(C), _row_spec(C)],
        out_specs=[_row_spec(C), _vec_spec(C), _vec_spec(C)],
        out_shape=[jax.ShapeDtypeStruct((S, C), F32), jax.ShapeDtypeStruct((1, C), F32), jax.ShapeDtypeStruct((1, C), F32)],
        compiler_params=_params(("arbitrary",)),
    )(c1, g, b, dc)


def _mix_call(proj, gate_col0, b_gate, y_a, y_c):
    S, D = y_a.shape
    w = GATE_COLS
    nc = D // w
    ga0, gc0 = gate_col0 // w, (gate_col0 + D) // w

    def body(ga_ref, gc_ref, ba_ref, bc_ref, ya_ref, yc_ref, o_ref):
        o_ref[...] = (_sigmoid(ga_ref[...] + ba_ref[...]) * ya_ref[...]
                      + _sigmoid(gc_ref[...] + bc_ref[...]) * yc_ref[...]).astype(BF16)

    tile = lambda off: pl.BlockSpec((GATE_ROWS, w), lambda i, j: (i, off + j))
    vec = lambda off: pl.BlockSpec((1, w), lambda i, j: (0, off + j))
    return pl.pallas_call(
        body, name="gate_mix", grid=(S // GATE_ROWS, nc),
        in_specs=[tile(ga0), tile(gc0), vec(0), vec(nc), tile(0), tile(0)],
        out_specs=tile(0), out_shape=jax.ShapeDtypeStruct((S, D), BF16),
        compiler_params=_params(("parallel", "parallel")),
    )(proj, proj, b_gate, b_gate, y_a, y_c)


def _mix_bwd_call(dmixed, proj, gate_col0, b_gate, y_a, y_c):
    S, D = y_a.shape
    w = GATE_COLS
    nc = D // w
    ga0, gc0 = gate_col0 // w, (gate_col0 + D) // w

    def body(dm_ref, ga_ref, gc_ref, ba_ref, bc_ref, ya_ref, yc_ref, dya_ref, dyc_ref, dga_ref, dgc_ref, dba_ref, dbc_ref):
        dm = dm_ref[...]
        sa = _sigmoid(ga_ref[...] + ba_ref[...])
        sc = _sigmoid(gc_ref[...] + bc_ref[...])
        dya_ref[...] = (dm * sa).astype(BF16)
        dyc_ref[...] = (dm * sc).astype(BF16)
        dga = dm * ya_ref[...] * sa * (1.0 - sa)
        dgc = dm * yc_ref[...] * sc * (1.0 - sc)
        dga_ref[...] = dga.astype(BF16)
        dgc_ref[...] = dgc.astype(BF16)
        pa = jnp.sum(dga, axis=0, keepdims=True)
        pc = jnp.sum(dgc, axis=0, keepdims=True)

        @pl.when(pl.program_id(1) == 0)
        def _():
            dba_ref[...] = pa
            dbc_ref[...] = pc

        @pl.when(pl.program_id(1) > 0)
        def _():
            dba_ref[...] += pa
            dbc_ref[...] += pc

    tile = lambda off: pl.BlockSpec((GATE_ROWS, w), lambda j, i: (i, off + j))
    vec = lambda off: pl.BlockSpec((1, w), lambda j, i: (0, off + j))
    return pl.pallas_call(
        body, name="gate_mix_bwd", grid=(nc, S // GATE_ROWS),
        in_specs=[tile(0), tile(ga0), tile(gc0), vec(0), vec(nc), tile(0), tile(0)],
        out_specs=[tile(0), tile(0), tile(0), tile(0), vec(0), vec(0)],
        out_shape=[jax.ShapeDtypeStruct((S, D), BF16)] * 4 + [
                   jax.ShapeDtypeStruct((1, D), F32), jax.ShapeDtypeStruct((1, D), F32)],
        compiler_params=_params(("parallel", "arbitrary")),
    )(dmixed, proj, proj, b_gate, b_gate, y_a, y_c)


def _res1_call(x, out, g_post, g_pre):
    S, D = x.shape

    def body(x_ref, o_ref, gp_ref, gq_ref, x1_ref, h2_ref):
        x1 = x_ref[...] + _rms(o_ref[...], gp_ref[...])
        x1_ref[...] = x1
        h2_ref[...] = _rms(x1, gq_ref[...]).astype(BF16)

    return pl.pallas_call(
        body, name="residual_mix", grid=(S // ROW_TILE,),
        in_specs=[_row_spec(D), _row_spec(D), _vec_spec(D), _vec_spec(D)],
        out_specs=[_row_spec(D), _row_spec(D)],
        out_shape=[jax.ShapeDtypeStruct((S, D), F32), jax.ShapeDtypeStruct((S, D), BF16)],
        compiler_params=_params(("parallel",)),
    )(x, out, g_post, g_pre)


def _loss_call(y, x1, g_post, target):
    S, D = y.shape

    def body(y_ref, x1_ref, g_ref, t_ref, loss_ref, dx_ref, dy_ref, dg_ref):
        yv, gv = y_ref[...], g_ref[...]
        err = x1_ref[...] + _rms(yv, gv) - t_ref[...]
        dx2 = err * (1.0 / D)
        dx_ref[...] = dx2
        dy, dg = _rms_bwd(yv, gv, dx2)
        dy_ref[...] = dy.astype(BF16)
        _accumulate(dg_ref, dg)
        part = 0.5 * jnp.sum(jnp.mean(err * err, axis=-1, keepdims=True), axis=0, keepdims=True)
        _accumulate(loss_ref, jnp.broadcast_to(part, (SUBLANES, LANES)))

    return pl.pallas_call(
        body, name="residual_ffn_loss", grid=(S // ROW_TILE,),
        in_specs=[_row_spec(D), _row_spec(D), _vec_spec(D), _row_spec(D)],
        out_specs=[pl.BlockSpec((SUBLANES, LANES), lambda i: (0, 0)), _row_spec(D), _row_spec(D), _vec_spec(D)],
        out_shape=[jax.ShapeDtypeStruct((SUBLANES, LANES), F32), jax.ShapeDtypeStruct((S, D), F32),
                   jax.ShapeDtypeStruct((S, D), BF16), jax.ShapeDtypeStruct((1, D), F32)],
        compiler_params=_params(("arbitrary",)),
    )(y, x1, g_post, target)


def _mid_bwd_call(x1, g_pre, dh2, dx2, out, g_post):
    S, D = x1.shape

    def body(x1_ref, gq_ref, dh_ref, dx2_ref, o_ref, gp_ref, dx1_ref, do_ref, dgq_ref, dgp_ref):
        d, dgq = _rms_bwd(x1_ref[...], gq_ref[...], dh_ref[...])
        dx1 = dx2_ref[...] + d
        dx1_ref[...] = dx1
        do, dgp = _rms_bwd(o_ref[...], gp_ref[...], dx1)
        do_ref[...] = do.astype(BF16)
        _accumulate(dgq_ref, dgq)
        _accumulate(dgp_ref, dgp)

    return pl.pallas_call(
        body, name="residual_mix_bwd", grid=(S // ROW_TILE,),
        in_specs=[_row_spec(D), _vec_spec(D), _row_spec(D), _row_spec(D), _row_spec(D), _vec_spec(D)],
        out_specs=[_row_spec(D), _row_spec(D), _vec_spec(D), _vec_spec(D)],
        out_shape=[jax.ShapeDtypeStruct((S, D), F32), jax.ShapeDtypeStruct((S, D), BF16)] + [jax.ShapeDtypeStruct((1, D), F32)] * 2,
        compiler_params=_params(("arbitrary",)),
    )(x1, g_pre, dh2, dx2, out, g_post)


def _in_bwd_call(x, g, dh1, dx1):
    S, D = x.shape

    def body(x_ref, g_ref, dh_ref, dx1_ref, gx_ref, dg_ref):
        d, dg = _rms_bwd(x_ref[...], g_ref[...], dh_ref[...])
        gx_ref[...] = dx1_ref[...] + d
        _accumulate(dg_ref, dg)

    return pl.pallas_call(
        body, name="rms_mix_pre_bwd", grid=(S // ROW_TILE,),
        in_specs=[_row_spec(D), _vec_spec(D), _row_spec(D), _row_spec(D)],
        out_specs=[_row_spec(D), _vec_spec(D)],
        out_shape=[jax.ShapeDtypeStruct((S, D), F32), jax.ShapeDtypeStruct((1, D), F32)],
        compiler_params=_params(("arbitrary",)),
    )(x, g, dh1, dx1)


def _bucket_table(dilation):
    qi = np.arange(SPAN)[:, None]
    ki = np.arange(2 * SPAN)[None, :]
    dist = np.maximum(qi + SPAN - ki, 0) * dilation
    max_exact = N_BUCKETS // 2
    d = np.maximum(dist, 1).astype(np.float64)
    large = max_exact + (np.log(d / max_exact) / math.log(MAX_DISTANCE / max_exact) * (N_BUCKETS - max_exact)).astype(np.int32)
    large = np.minimum(large, N_BUCKETS - 1)
    return np.where(dist < max_exact, dist, large).astype(np.int32)


def _bucket_tables():
    return jnp.asarray(np.stack([_bucket_table(r) for _, r in DILATED_PATTERNS]))


def _bias_table_call(rel_bias, buckets):
    def body(rb_ref, bk_ref, o_ref):
        for h in range(N_HEADS):
            bk = bk_ref[h // HEADS_PER_GROUP]

            def step(b, acc):
                return jnp.where(bk == b, rb_ref[b, h], acc)

            o_ref[h] = lax.fori_loop(0, N_BUCKETS, step, jnp.zeros((SPAN, 2 * SPAN), F32))

    return pl.pallas_call(
        body, name="rel_bias_table",
        in_specs=[pl.BlockSpec(memory_space=pltpu.SMEM), pl.BlockSpec(memory_space=pltpu.VMEM)],
        out_specs=pl.BlockSpec(memory_space=pltpu.VMEM),
        out_shape=jax.ShapeDtypeStruct((N_HEADS, SPAN, 2 * SPAN), F32),
    )(rel_bias, buckets)


def _bias_grad_call(dbias, buckets):
    def body(db_ref, bk_ref, o_ref, rows_ref):
        for h in range(N_HEADS):
            bk = bk_ref[h // HEADS_PER_GROUP]
            dv = db_ref[h]

            def step(b, carry):
                rows_ref[h, b] = jnp.sum(jnp.where(bk == b, dv, 0.0), axis=0, keepdims=True)
                return carry

            lax.fori_loop(0, N_BUCKETS, step, 0)
        o_ref[...] = jnp.sum(rows_ref[...], axis=-1, keepdims=True)

    out = pl.pallas_call(
        body, name="rel_bias_grad",
        in_specs=[pl.BlockSpec(memory_space=pltpu.VMEM), pl.BlockSpec(memory_space=pltpu.VMEM)],
        out_specs=pl.BlockSpec(memory_space=pltpu.VMEM),
        out_shape=jax.ShapeDtypeStruct((N_HEADS, N_BUCKETS, 1, 1), F32),
        scratch_shapes=[pltpu.VMEM((N_HEADS, N_BUCKETS, 1, 2 * SPAN), F32)],
    )(dbias, buckets)
    return out.reshape(N_HEADS, N_BUCKETS).T


def _dot_nt(a, b):
    return lax.dot_general(a, b, (((1,), (1,)), ((), ())), preferred_element_type=F32)


def _dot_nn(a, b):
    return lax.dot_general(a, b, (((1,), (0,)), ((), ())), preferred_element_type=F32)


def _dot_tn(a, b):
    return lax.dot_general(a, b, (((0,), (0,)), ((), ())), preferred_element_type=F32)


def _band_masks(n, nb):
    qi = lax.broadcasted_iota(jnp.int32, (SPAN, SPAN), 0)
    ki = lax.broadcasted_iota(jnp.int32, (SPAN, SPAN), 1)
    prev_ok = jnp.logical_and(ki >= qi, n > 0)
    cur_ok = ki <= qi
    next_ok = jnp.logical_and(ki >= qi, n < nb - 1)
    return prev_ok, cur_ok, next_ok


def _wide_band_mask(n):
    qi = lax.broadcasted_iota(jnp.int32, (SPAN, 2 * SPAN), 0)
    ki = lax.broadcasted_iota(jnp.int32, (SPAN, 2 * SPAN), 1)
    prev_ok = jnp.logical_and(jnp.logical_and(ki < SPAN, ki >= qi), n > 0)
    cur_ok = jnp.logical_and(ki >= SPAN, ki - SPAN <= qi)
    return jnp.logical_or(prev_ok, cur_ok)


def _attn_plan(S, group):
    r = DILATED_PATTERNS[group][1]
    hp, per = (HEADS_PER_GROUP, 1) if r == 1 else (2, 4)
    return r, S // (r * SPAN), hp, per


def _residue_rows(rho, r):
    return slice(None) if r == 1 else pl.ds(rho, SPAN, stride=r)


def _for_residues(r, per, fn):
    if r == per:
        for u in range(per):
            fn(u)
        return

    def step(i, carry):
        for u in range(per):
            fn(i * per + u)
        return carry

    lax.fori_loop(0, r // per, step, 0)


def _attn_fwd_call(proj, bias, group):
    S = proj.shape[0]
    r, nb, hp, per = _attn_plan(S, group)
    scale = HEAD_DIM ** -0.5
    kinds = ("q", "kp", "kc", "vp", "vc") if nb > 1 else ("q", "kc", "vc")

    def body(*refs):
        ins = {kind: refs[i * hp:(i + 1) * hp] for i, kind in enumerate(kinds)}
        b_ref, o_ref, lse_ref = refs[len(kinds) * hp:]
        n = pl.program_id(1)
        prev_ok, cur_ok, _ = _band_masks(n, nb)

        band_ok = _wide_band_mask(n) if nb > 1 else cur_ok

        def residue(rho):
            rows = _residue_rows(rho, r)
            for j in range(hp):
                get = lambda kind: ins[kind][j][rows, :].astype(BF16)
                q = get("q")
                if nb > 1:
                    keys, vals, bias_j = jnp.concatenate([get("kp"), get("kc")], axis=0), jnp.concatenate([get("vp"), get("vc")], axis=0), b_ref[j]
                else:
                    keys, vals, bias_j = get("kc"), get("vc"), b_ref[j, :, SPAN:]
                s = jnp.where(band_ok, _dot_nt(q, keys) * scale + bias_j, NEG_INF)
                m = jnp.max(s, axis=-1, keepdims=True)
                p = jnp.exp(s - m)
                den = jnp.sum(p, axis=-1, keepdims=True)
                o_ref[j, rows, :] = _dot_nn(p.astype(BF16), vals) / den
                lse_ref[j, rows, :] = jnp.broadcast_to(m + jnp.log(den), (SPAN, HEAD_DIM))

        _for_residues(r, per, residue)

    in_specs = [_head_spec(r, nb, hp, kind, group, jj) for kind in kinds for jj in range(hp)]
    in_specs.append(pl.BlockSpec((hp, SPAN, 2 * SPAN), lambda j, n: (group * (HEADS_PER_GROUP // hp) + j, 0, 0)))
    out = pl.BlockSpec((hp, r * SPAN, HEAD_DIM), lambda j, n: (j, n, 0))
    return pl.pallas_call(
        body, name=f"attn_fwd_g{group}", grid=(HEADS_PER_GROUP // hp, nb),
        in_specs=in_specs, out_specs=[out] * 2,
        out_shape=[jax.ShapeDtypeStruct((HEADS_PER_GROUP, S, HEAD_DIM), F32)] * 2,
        compiler_params=_params(("parallel", "parallel"), VMEM_LIMIT),
    )(*([proj] * (len(in_specs) - 1)), bias)


_PROJ_PART = dict(q=0, qn=0, kp=1, kc=1, vp=2, vc=2)


def _head_spec(r, nb, hp, kind, group, jj):
    if kind in _PROJ_PART:
        base = (_PROJ_PART[kind] * N_GROUPS + group) * HEADS_PER_GROUP
    else:
        base = 0
    if kind.endswith("p"):
        row = lambda n: jnp.maximum(n - 1, 0)
    elif kind.endswith("n"):
        row = lambda n: jnp.minimum(n + 1, nb - 1)
    else:
        row = lambda n: n
    return pl.BlockSpec((r * SPAN, HEAD_DIM), lambda j, n: (row(n), base + j * hp + jj))


def _attn_merge_call(parts):
    S = parts[0].shape[1]

    def body(o1, s1, o2, s2, o3, s3, a_ref, ab_ref, lse_ref):
        for j in range(HEADS_PER_GROUP):
            sl = slice(j * HEAD_DIM, (j + 1) * HEAD_DIM)
            mx = jnp.maximum(jnp.maximum(s1[j], s2[j]), s3[j])
            w1 = jnp.exp(s1[j] - mx)
            w2 = jnp.exp(s2[j] - mx)
            w3 = jnp.exp(s3[j] - mx)
            den = w1 + w2 + w3
            a = (w1 * o1[j] + w2 * o2[j] + w3 * o3[j]) / den
            a_ref[:, sl] = a
            ab_ref[:, sl] = a.astype(BF16)
            lse_ref[:, sl] = mx + jnp.log(den)

    heads = pl.BlockSpec((HEADS_PER_GROUP, ROW_TILE, HEAD_DIM), lambda i: (0, i, 0))
    return pl.pallas_call(
        body, name="attn_merge", grid=(S // ROW_TILE,),
        in_specs=[heads] * 6, out_specs=[_row_spec(GROUP_WIDTH)] * 3,
        out_shape=[jax.ShapeDtypeStruct((S, GROUP_WIDTH), F32), jax.ShapeDtypeStruct((S, GROUP_WIDTH), BF16),
                   jax.ShapeDtypeStruct((S, GROUP_WIDTH), F32)],
        compiler_params=_params(("parallel",)),
    )(*parts)


def _attn_delta_call(a, da):
    S = a.shape[0]

    def body(a_ref, da_ref, d_ref):
        for j in range(HEADS_PER_GROUP):
            sl = slice(j * HEAD_DIM, (j + 1) * HEAD_DIM)
            d = jnp.sum(a_ref[:, sl] * da_ref[:, sl], axis=-1, keepdims=True)
            d_ref[:, sl] = jnp.broadcast_to(d, (ROW_TILE, HEAD_DIM))

    return pl.pallas_call(
        body, name="attn_delta", grid=(S // ROW_TILE,),
        in_specs=[_row_spec(GROUP_WIDTH)] * 2, out_specs=_row_spec(GROUP_WIDTH),
        out_shape=jax.ShapeDtypeStruct((S, GROUP_WIDTH), F32),
        compiler_params=_params(("parallel",)),
    )(a, da)


def _attn_bwd_call(proj, bias, da, lse, delta, group):
    S = proj.shape[0]
    r, nb, hp, per = _attn_plan(S, group)
    scale = HEAD_DIM ** -0.5
    kinds = ("q", "qn", "kp", "kc", "vp", "vc", "da", "dan", "lse", "lsen", "dl", "dln") if nb > 1 else ("q", "kc", "vc", "da", "lse", "dl")
    source = dict(da=da, dan=da, lse=lse, lsen=lse, dl=delta, dln=delta)

    def body(*refs):
        ins = {kind: refs[i * hp:(i + 1) * hp] for i, kind in enumerate(kinds)}
        b_ref, dq_ref, dk_ref, dv_ref, db_ref = refs[len(kinds) * hp:]
        n = pl.program_id(1)
        prev_ok, cur_ok, next_ok = _band_masks(n, nb)

        @pl.when(n == 0)
        def _():
            db_ref[...] = jnp.zeros_like(db_ref)

        band_ok = _wide_band_mask(n) if nb > 1 else cur_ok

        def residue(rho):
            rows = _residue_rows(rho, r)
            for j in range(hp):
                get = lambda kind: ins[kind][j][rows, :]
                q = get("q").astype(BF16)
                kc = get("kc").astype(BF16)
                vc = get("vc").astype(BF16)
                dav = get("da").astype(BF16)
                lse_q, dl_q = get("lse"), get("dl")
                if nb == 1:
                    pc = jnp.exp(jnp.where(cur_ok, _dot_nt(q, kc) * scale + b_ref[j, :, SPAN:], NEG_INF) - lse_q)
                    dsc = pc * (_dot_nt(dav, vc) - dl_q)
                    dsc_b = dsc.astype(BF16)
                    dq = _dot_nn(dsc_b, kc)
                    dk = _dot_tn(dsc_b, q)
                    dv = _dot_tn(pc.astype(BF16), dav)
                    db_ref[j, :, SPAN:] += dsc
                else:
                    qn = get("qn").astype(BF16)
                    dan = get("dan").astype(BF16)
                    keys = jnp.concatenate([get("kp").astype(BF16), kc], axis=0)
                    vals = jnp.concatenate([get("vp").astype(BF16), vc], axis=0)
                    wide = lambda t: jnp.concatenate([t, t], axis=1)
                    p = jnp.exp(jnp.where(band_ok, _dot_nt(q, keys) * scale + b_ref[j], NEG_INF) - wide(lse_q))
                    ds = p * (_dot_nt(dav, vals) - wide(dl_q))
                    dq = _dot_nn(ds.astype(BF16), keys)
                    db_ref[j] += ds
                    pn = jnp.exp(jnp.where(next_ok, _dot_nt(qn, kc) * scale + b_ref[j, :, :SPAN], NEG_INF) - get("lsen"))
                    dsn = pn * (_dot_nt(dan, vc) - get("dln"))
                    both = lambda cur_part, next_part: jnp.concatenate([cur_part.astype(BF16), next_part.astype(BF16)], axis=0)
                    dk = _dot_tn(both(ds[:, SPAN:], dsn), jnp.concatenate([q, qn], axis=0))
                    dv = _dot_tn(both(p[:, SPAN:], pn), jnp.concatenate([dav, dan], axis=0))
                dq_ref[j, rows, :] = dq * scale
                dk_ref[j, rows, :] = dk * scale
                dv_ref[j, rows, :] = dv

        _for_residues(r, per, residue)

    per_group = HEADS_PER_GROUP // hp
    band = (hp, SPAN, 2 * SPAN)
    in_specs = [_head_spec(r, nb, hp, kind, group, jj) for kind in kinds for jj in range(hp)]
    in_specs.append(pl.BlockSpec(band, lambda j, n: (group * per_group + j, 0, 0)))
    operands = [source.get(kind, proj) for kind in kinds for _ in range(hp)] + [bias]
    out = pl.BlockSpec((hp, r * SPAN, HEAD_DIM), lambda j, n: (j, n, 0))
    return pl.pallas_call(
        body, name=f"attn_bwd_g{group}", grid=(per_group, nb),
        in_specs=in_specs,
        out_specs=[out] * 3 + [pl.BlockSpec(band, lambda j, n: (j, 0, 0))],
        out_shape=[jax.ShapeDtypeStruct((HEADS_PER_GROUP, S, HEAD_DIM), F32)] * 3
        + [jax.ShapeDtypeStruct((HEADS_PER_GROUP, SPAN, 2 * SPAN), F32)],
        compiler_params=_params(("parallel", "arbitrary"), VMEM_LIMIT),
    )(*operands)


def _dproj_call(dqkv, tails):
    S = tails[0].shape[0]
    width = len(dqkv) * GROUP_WIDTH + sum(t.shape[1] for t in tails)

    def body(*refs):
        o_ref = refs[-1]
        col = 0
        for ref in refs[:len(dqkv)]:
            for j in range(HEADS_PER_GROUP):
                o_ref[:, col:col + HEAD_DIM] = ref[j].astype(BF16)
                col += HEAD_DIM
        for ref in refs[len(dqkv):-1]:
            o_ref[:, col:col + ref.shape[1]] = ref[...]
            col += ref.shape[1]

    heads = pl.BlockSpec((HEADS_PER_GROUP, ROW_TILE, HEAD_DIM), lambda i: (0, i, 0))
    return pl.pallas_call(
        body, name="dproj_assemble", grid=(S // ROW_TILE,),
        in_specs=[heads] * len(dqkv) + [_row_spec(t.shape[1]) for t in tails],
        out_specs=_row_spec(width), out_shape=jax.ShapeDtypeStruct((S, width), BF16),
        compiler_params=_params(("parallel",)),
    )(*dqkv, *tails)


def _tap_rows(xpad_ref, t0, k, width, pad):
    return xpad_ref[pl.ds(t0 + (pad - (width - 1 - k)), TIME_BLOCK), :]


def _conv_block(xpad_ref, t0, w_ref, width, pad):
    acc = None
    for k in range(width):
        term = w_ref[k:k + 1, :] * _tap_rows(xpad_ref, t0, k, width, pad)
        acc = term if acc is None else acc + term
    return acc


def _conv_transpose_block(dpad_ref, t0, w_ref, width):
    acc = None
    for k in range(width):
        term = w_ref[k:k + 1, :] * dpad_ref[pl.ds(t0 + (width - 1 - k), TIME_BLOCK), :]
        acc = term if acc is None else acc + term
    return acc


def _conv_weight_grad(xpad_ref, t0, dy, dw_ref, width, pad):
    for k in range(width):
        dw_ref[k:k + 1, :] += jnp.sum(dy * _tap_rows(xpad_ref, t0, k, width, pad), axis=0, keepdims=True)


def _time_loop(S, step, skip_first=0, skip_last=0):
    def it(tb, carry):
        step(pl.multiple_of(tb * TIME_BLOCK, TIME_BLOCK))
        return carry

    lax.fori_loop(skip_first, S // TIME_BLOCK - skip_last, it, 0)


def _fill_head(head_ref, x_ref, pad):
    head_ref[0:pad, :] = jnp.zeros((pad, LANES), F32)
    head_ref[pad:, :] = x_ref[0:TIME_BLOCK, :]


def _fill_tail(tail_ref, x_ref, pad):
    S = x_ref.shape[0]
    tail_ref[0:TIME_BLOCK, :] = x_ref[S - TIME_BLOCK:S, :]
    tail_ref[TIME_BLOCK:, :] = jnp.zeros((pad, LANES), F32)


def _conv_fwd_call(proj, col0, w, b):
    S = proj.shape[0]
    C = w.shape[1]
    nt = C // LANES
    v0, g0 = col0 // LANES, (col0 + C) // LANES

    def body(val_ref, gate_ref, w_ref, b_ref, o_ref, pad_ref):
        pad_ref[0:CONV_PAD, :] = jnp.zeros((CONV_PAD, LANES), F32)
        pad_ref[CONV_PAD:, :] = val_ref[...] * _sigmoid(gate_ref[...])

        def step(t0):
            o_ref[pl.ds(t0, TIME_BLOCK), :] = _conv_block(pad_ref, t0, w_ref, CONV_WIDTH, CONV_PAD) + b_ref[...]

        _time_loop(S, step)

    seq = lambda off: pl.BlockSpec((S, LANES), lambda i: (0, off + i))
    return pl.pallas_call(
        body, name="conv_module", grid=(nt,),
        in_specs=[seq(v0), seq(g0), pl.BlockSpec((CONV_WIDTH, LANES), lambda i: (0, i)), pl.BlockSpec((1, LANES), lambda i: (0, i))],
        out_specs=seq(0), out_shape=jax.ShapeDtypeStruct((S, C), F32),
        scratch_shapes=[pltpu.VMEM((S + CONV_PAD, LANES), F32)],
        compiler_params=_params(("parallel",)),
    )(proj, proj, w, b)


def _conv_bwd_call(proj, col0, w, dc1):
    S = proj.shape[0]
    C = w.shape[1]
    nt = C // LANES
    v0, g0 = col0 // LANES, (col0 + C) // LANES

    def body(val_ref, gate_ref, w_ref, dy_ref, dval_ref, dgate_ref, dw_ref, db_ref, xpad_ref, tail_ref, dwacc_ref):
        xpad_ref[0:CONV_PAD, :] = jnp.zeros((CONV_PAD, LANES), F32)
        xpad_ref[CONV_PAD:, :] = val_ref[...] * _sigmoid(gate_ref[...])
        _fill_tail(tail_ref, dy_ref, CONV_PAD)
        dwacc_ref[...] = jnp.zeros_like(dwacc_ref)

        def block(t0, dy_src, dy_t0):
            rows = pl.ds(t0, TIME_BLOCK)
            _conv_weight_grad(xpad_ref, t0, dy_ref[rows, :], dwacc_ref, CONV_WIDTH, CONV_PAD)
            dc0 = _conv_transpose_block(dy_src, dy_t0, w_ref, CONV_WIDTH)
            sg = _sigmoid(gate_ref[rows, :])
            dval_ref[rows, :] = (dc0 * sg).astype(BF16)
            dgate_ref[rows, :] = (dc0 * val_ref[rows, :] * sg * (1.0 - sg)).astype(BF16)

        _time_loop(S, lambda t0: block(t0, dy_ref, t0), skip_last=1)
        block(S - TIME_BLOCK, tail_ref, 0)
        dw_ref[...] = dwacc_ref[...]
        db_ref[...] = jnp.sum(dy_ref[...], axis=0, keepdims=True)

    seq = lambda off: pl.BlockSpec((S, LANES), lambda i: (0, off + i))
    return pl.pallas_call(
        body, name="conv_module_bwd", grid=(nt,),
        in_specs=[seq(v0), seq(g0), pl.BlockSpec((CONV_WIDTH, LANES), lambda i: (0, i)), seq(0)],
        out_specs=[seq(0), seq(0), pl.BlockSpec((CONV_PAD, LANES), lambda i: (0, i)), pl.BlockSpec((1, LANES), lambda i: (0, i))],
        out_shape=[jax.ShapeDtypeStruct((S, C), BF16), jax.ShapeDtypeStruct((S, C), BF16),
                   jax.ShapeDtypeStruct((CONV_PAD, C), F32), jax.ShapeDtypeStruct((1, C), F32)],
        scratch_shapes=[pltpu.VMEM((S + CONV_PAD, LANES), F32), pltpu.VMEM((TIME_BLOCK + CONV_PAD, LANES), F32),
                        pltpu.VMEM((CONV_PAD, LANES), F32)],
        compiler_params=_params(("parallel",)),
    )(proj, proj, w, dc1)


def _ffn_fwd_call(u, w, b):
    S, C2 = u.shape
    C = C2 // 2
    nt = C // LANES

    def body(ug_ref, uv_ref, wg_ref, wv_ref, bg_ref, bv_ref, f_ref, hg_ref, hv_ref):
        _fill_head(hg_ref, ug_ref, FFN_PAD)
        _fill_head(hv_ref, uv_ref, FFN_PAD)

        def block(t0, xg_ref, xv_ref, x_t0, pad):
            cg = _conv_block(xg_ref, x_t0, wg_ref, FFN_CONV_WIDTH, pad) + bg_ref[...]
            cv = _conv_block(xv_ref, x_t0, wv_ref, FFN_CONV_WIDTH, pad) + bv_ref[...]
            f_ref[pl.ds(t0, TIME_BLOCK), :] = (_gelu(cg) * cv).astype(BF16)

        block(0, hg_ref, hv_ref, 0, FFN_PAD)
        _time_loop(S, lambda t0: block(t0, ug_ref, uv_ref, t0, 0), skip_first=1)

    seq = lambda off: pl.BlockSpec((S, LANES), lambda i: (0, off + i))
    wsp = lambda off: pl.BlockSpec((FFN_CONV_WIDTH, LANES), lambda i: (0, off + i))
    bsp = lambda off: pl.BlockSpec((1, LANES), lambda i: (0, off + i))
    return pl.pallas_call(
        body, name="ffn_conv_geglu", grid=(nt,),
        in_specs=[seq(0), seq(nt), wsp(0), wsp(nt), bsp(0), bsp(nt)],
        out_specs=seq(0), out_shape=jax.ShapeDtypeStruct((S, C), BF16),
        scratch_shapes=[pltpu.VMEM((FFN_PAD + TIME_BLOCK, LANES), F32)] * 2,
        compiler_params=_params(("parallel",)),
    )(u, u, w, w, b, b)


def _ffn_bwd_call(u, w, b, df):
    S, C2 = u.shape
    C = C2 // 2
    nt = C // LANES

    def body(ug_ref, uv_ref, wg_ref, wv_ref, bg_ref, bv_ref, df_ref,
             du_ref, dwg_ref, dwv_ref, dbg_ref, dbv_ref,
             hg_ref, hv_ref, dg_ref, dv_ref, dwg_acc, dwv_acc, dbg_acc, dbv_acc):
        zeros = jnp.zeros((FFN_PAD, LANES), F32)
        _fill_head(hg_ref, ug_ref, FFN_PAD)
        _fill_head(hv_ref, uv_ref, FFN_PAD)
        dg_ref[S:, :] = zeros
        dv_ref[S:, :] = zeros
        dwg_acc[...] = jnp.zeros_like(dwg_acc)
        dwv_acc[...] = jnp.zeros_like(dwv_acc)
        dbg_acc[...] = jnp.zeros_like(dbg_acc)
        dbv_acc[...] = jnp.zeros_like(dbv_acc)

        def first(t0, xg_ref, xv_ref, x_t0, pad):
            rows = pl.ds(t0, TIME_BLOCK)
            cg = _conv_block(xg_ref, x_t0, wg_ref, FFN_CONV_WIDTH, pad) + bg_ref[...]
            cv = _conv_block(xv_ref, x_t0, wv_ref, FFN_CONV_WIDTH, pad) + bv_ref[...]
            dfb = df_ref[rows, :]
            gelu, gelu_grad = _gelu_and_grad(cg)
            dcg = dfb * cv * gelu_grad
            dcv = dfb * gelu
            dg_ref[rows, :] = dcg
            dv_ref[rows, :] = dcv
            _conv_weight_grad(xg_ref, x_t0, dcg, dwg_acc, FFN_CONV_WIDTH, pad)
            _conv_weight_grad(xv_ref, x_t0, dcv, dwv_acc, FFN_CONV_WIDTH, pad)
            dbg_acc[...] += jnp.sum(dcg, axis=0, keepdims=True)
            dbv_acc[...] += jnp.sum(dcv, axis=0, keepdims=True)

        def second(t0):
            rows = pl.ds(t0, TIME_BLOCK)
            du_ref[0, rows, :] = _conv_transpose_block(dg_ref, t0, wg_ref, FFN_CONV_WIDTH).astype(BF16)
            du_ref[1, rows, :] = _conv_transpose_block(dv_ref, t0, wv_ref, FFN_CONV_WIDTH).astype(BF16)

        first(0, hg_ref, hv_ref, 0, FFN_PAD)
        _time_loop(S, lambda t0: first(t0, ug_ref, uv_ref, t0, 0), skip_first=1)
        _time_loop(S, second)
        dwg_ref[...] = dwg_acc[...]
        dwv_ref[...] = dwv_acc[...]
        dbg_ref[...] = dbg_acc[...]
        dbv_ref[...] = dbv_acc[...]

    seq = lambda off: pl.BlockSpec((S, LANES), lambda i: (0, off + i))
    wsp = lambda off: pl.BlockSpec((FFN_CONV_WIDTH, LANES), lambda i: (0, off + i))
    bsp = lambda off: pl.BlockSpec((1, LANES), lambda i: (0, off + i))
    return pl.pallas_call(
        body, name="ffn_conv_geglu_bwd", grid=(nt,),
        in_specs=[seq(0), seq(nt), wsp(0), wsp(nt), bsp(0), bsp(nt), seq(0)],
        out_specs=[pl.BlockSpec((2, S, LANES), lambda i: (0, 0, i)),
                   pl.BlockSpec((SUBLANES, LANES), lambda i: (0, i)), pl.BlockSpec((SUBLANES, LANES), lambda i: (0, i)),
                   bsp(0), bsp(0)],
        out_shape=[jax.ShapeDtypeStruct((2, S, C), BF16)] + [jax.ShapeDtypeStruct((SUBLANES, C), F32)] * 2
        + [jax.ShapeDtypeStruct((1, C), F32)] * 2,
        scratch_shapes=[pltpu.VMEM((FFN_PAD + TIME_BLOCK, LANES), F32)] * 2 + [pltpu.VMEM((S + FFN_PAD, LANES), F32)] * 2
        + [pltpu.VMEM((SUBLANES, LANES), F32)] * 2
        + [pltpu.VMEM((1, LANES), F32)] * 2,
        compiler_params=_params(("parallel",)),
    )(u, u, w, w, b, b, df)


def _adamw(w_ref, g_ref, m_ref, v_ref, d_ref, mo_ref, vo_ref):
    gv = g_ref[...]
    mn = ADAM_B1 * m_ref[...] + (1.0 - ADAM_B1) * gv
    vn = ADAM_B2 * v_ref[...] + (1.0 - ADAM_B2) * (gv * gv)
    mo_ref[...] = mn
    vo_ref[...] = vn
    m_hat = mn * (1.0 / (1.0 - ADAM_B1 ** ADAM_STEP))
    v_hat = vn * (1.0 / (1.0 - ADAM_B2 ** ADAM_STEP))
    d_ref[...] = -ADAM_LR * (m_hat / (jnp.sqrt(v_hat) + ADAM_EPS) + ADAM_WD * w_ref[...])


def _adamw_call(w, g, m, v, name):
    R, C = w.shape
    tr = _row_tile(R, C)

    def body(w_ref, g_ref, m_ref, v_ref, go_ref, d_ref, mo_ref, vo_ref):
        go_ref[...] = g_ref[...]
        _adamw(w_ref, g_ref, m_ref, v_ref, d_ref, mo_ref, vo_ref)

    spec = pl.BlockSpec((tr, C), lambda i: (i, 0))
    return pl.pallas_call(
        body, name=name, grid=(R // tr,),
        in_specs=[spec] * 4, out_specs=[spec] * 4,
        out_shape=[jax.ShapeDtypeStruct((R, C), F32)] * 4,
        compiler_params=_params(("parallel",)),
    )(w, g, m, v)


def _adamw_small_call(ws, gs, ms, vs):
    n = len(ws)

    def body(*refs):
        w_refs, g_refs, m_refs, v_refs, d_refs, mo_refs, vo_refs = (refs[i * n:(i + 1) * n] for i in range(7))
        for i in range(n):
            _adamw(w_refs[i], g_refs[i], m_refs[i], v_refs[i], d_refs[i], mo_refs[i], vo_refs[i])

    whole = pl.BlockSpec(memory_space=pltpu.VMEM)
    outs = pl.pallas_call(
        body, name="adamw_small",
        in_specs=[whole] * (4 * n), out_specs=[whole] * (3 * n),
        out_shape=[jax.ShapeDtypeStruct(w.shape, F32) for w in ws] * 3,
    )(*ws, *gs, *ms, *vs)
    return outs[:n], outs[n:2 * n], outs[2 * n:]


def _position():
    return lax.axis_index("x"), lax.axis_index("y"), lax.axis_index("c")


def _chip_peers(x, y):
    return [(x, 1 - y), (1 - x, y), (1 - x, 1 - y)]


def _half_rows(ref, core, rows):
    h = rows // 2
    start = pl.multiple_of(core * h, 16)
    return ref.at[pl.ds(start, h), :] if len(ref.shape) == 2 else ref.at[:, pl.ds(start, h), :]


def _shard_half(ref, shard, core, rows):
    h = rows // 2
    return ref.at[shard, pl.ds(pl.multiple_of(core * h, 16), h), :]


ANY = pl.BlockSpec(memory_space=pl.ANY)


def _first_hop_copies(srcs, lands):
    x, y, c = _position()
    chip = 2 * x + y
    targets = [(px, py, c) for px, py in _chip_peers(x, y)] + [(x, y, 1 - c)]
    rows = srcs[0].shape[0]
    out = []
    for i, (s, l) in enumerate(zip(srcs, lands)):
        for k, dev in enumerate(targets):
            if i == 0 and k < 3:
                out.append((_half_rows(s, c, rows), _shard_half(l, chip, c, rows), dev, k))
            else:
                out.append((s, l.at[chip], dev, len(targets) * i + k))
    return out


def _second_hop_copies(srcs, lands):
    x, y, c = _position()
    rows = lands[0].shape[1]
    out = []
    for k, (px, py) in enumerate(_chip_peers(x, y)):
        half = _shard_half(lands[0], 2 * px + py, c, rows)
        out.append((half, half, (x, y, 1 - c), k))
    return out


HBM_SPEC = pl.BlockSpec(memory_space=pltpu.HBM)
SEM_SPEC = pl.BlockSpec(memory_space=pltpu.SEMAPHORE)
DATAFLOW = pltpu.SideEffectType.DATAFLOW_SIDE_EFFECTING


def _in_hbm(a):
    return pltpu.with_memory_space_constraint(a, pltpu.HBM)


def _split_start(name, groups, after, carry=None):
    spans, arrays = [], []
    for srcs, lands, _, _ in groups:
        spans.append((len(arrays), len(srcs), len(lands)))
        arrays += list(srcs) + list(lands)
    if carry is not None:
        arrays.append(carry)
    na, ng = len(arrays), len(groups)

    def body(*refs):
        sems, token = refs[na + 1:na + 1 + 2 * ng], refs[-1]
        for g, (_, _, _, copies) in enumerate(groups):
            off, ns, nl = spans[g]
            for src, dst, dev, idx in copies(refs[off:off + ns], refs[off + ns:off + ns + nl]):
                pltpu.make_async_remote_copy(src_ref=src, dst_ref=dst, send_sem=sems[2 * g].at[idx], recv_sem=sems[2 * g + 1].at[idx],
                                             device_id=dev, device_id_type=MESH).start()
        token[...] = jnp.zeros_like(token)

    outs = pl.pallas_call(
        body, name=name,
        in_specs=[HBM_SPEC] * na + [ANY],
        out_specs=[SEM_SPEC] * (2 * ng) + [HBM_SPEC] * na + [pl.BlockSpec(memory_space=pltpu.VMEM)],
        out_shape=[pltpu.SemaphoreType.DMA((n_sems,)) for _, _, n_sems, _ in groups for _ in range(2)]
        + [pltpu.HBM(a.shape, a.dtype) for a in arrays] + [jax.ShapeDtypeStruct((SUBLANES, LANES), F32)],
        input_output_aliases={i: 2 * ng + i for i in range(na)},
        compiler_params=pltpu.CompilerParams(has_side_effects=DATAFLOW),
    )(*[_in_hbm(a) for a in arrays], after)
    started = []
    for g, (off, ns, nl) in enumerate(spans):
        thru = outs[2 * ng + off:2 * ng + off + ns + nl]
        started.append(dict(send=outs[2 * g], recv=outs[2 * g + 1], srcs=list(thru[:ns]), lands=list(thru[ns:]),
                            tile=outs[-1], token=outs[-1][0, 0], carry=None if carry is None else outs[2 * ng + na - 1]))
    return started


def _split_wait(name, started, copies, after):
    n, m = len(started["srcs"]), len(started["lands"])

    def body(*refs):
        src_refs, land_refs = refs[:n], refs[n:n + m]
        send_sem, recv_sem = refs[n + m], refs[n + m + 1]
        for src, dst, dev, idx in copies(src_refs, land_refs):
            cp = pltpu.make_async_remote_copy(src_ref=src, dst_ref=dst, send_sem=send_sem.at[idx], recv_sem=recv_sem.at[idx],
                                              device_id=dev, device_id_type=MESH)
            cp.wait_send()
            cp.wait_recv()

    arrays = started["srcs"] + started["lands"]
    outs = pl.pallas_call(
        body, name=name,
        in_specs=[HBM_SPEC] * (n + m) + [SEM_SPEC, SEM_SPEC, ANY],
        out_specs=[HBM_SPEC] * (n + m),
        out_shape=[pltpu.HBM(a.shape, a.dtype) for a in arrays],
        input_output_aliases={i: i for i in range(n + m)},
        compiler_params=pltpu.CompilerParams(has_side_effects=DATAFLOW),
    )(*arrays, started["send"], started["recv"], after)
    return list(outs)


def _gather_copies(srcs, lands):
    x, y, c = _position()
    chip = 2 * x + y
    targets = [(px, py, c) for px, py in _chip_peers(x, y)] + [(x, y, 1 - c)]
    return [(s, l.at[chip], dev, len(targets) * i + k) for i, (s, l) in enumerate(zip(srcs, lands)) for k, dev in enumerate(targets)]


def _sibling_copies(srcs, lands):
    x, y, c = _position()
    return [(_half_rows(srcs[0], 1 - c, srcs[0].shape[1]), lands[0], (x, y, 1 - c), 0)]


def _exchange_copies(srcs, lands):
    x, y, c = _position()
    return [(srcs[0].at[2 * px + py], lands[0].at[k], (px, py, c), k) for k, (px, py) in enumerate(_chip_peers(x, y))]


def _pair_sum_call(grad, recv, core, name):
    _, h, B = recv.shape
    tr = _row_tile(h, B)

    def body(core_ref, g_ref, r_ref, o_ref, ob_ref):
        s = g_ref[...] + r_ref[...]
        o_ref[...] = s
        ob_ref[...] = s.astype(BF16)

    g_spec = pl.BlockSpec((None, tr, B), lambda q, i, core_ref: (q, core_ref[0] * (h // tr) + i, 0))
    spec = pl.BlockSpec((None, tr, B), lambda q, i, core_ref: (q, i, 0))
    return pl.pallas_call(
        body, name=name,
        grid_spec=pltpu.PrefetchScalarGridSpec(num_scalar_prefetch=1, grid=(N_CHIPS, h // tr), in_specs=[g_spec, spec],
                                               out_specs=[spec, spec]),
        out_shape=[jax.ShapeDtypeStruct(recv.shape, F32), jax.ShapeDtypeStruct(recv.shape, BF16)],
        compiler_params=_params(("parallel", "parallel")),
    )(core, grad, recv)


def _chip_sum_call(partial, recv, chip_core, name):
    _, h, B = recv.shape
    tr = _row_tile(h, B)

    def body(cc_ref, p_ref, r_ref, o_ref):
        o_ref[...] = ((p_ref[...] + r_ref[0].astype(F32)) + r_ref[1].astype(F32)) + r_ref[2].astype(F32)

    return pl.pallas_call(
        body, name=name,
        grid_spec=pltpu.PrefetchScalarGridSpec(
            num_scalar_prefetch=1, grid=(h // tr,),
            in_specs=[pl.BlockSpec((None, tr, B), lambda i, cc_ref: (cc_ref[0], i, 0)),
                      pl.BlockSpec((3, tr, B), lambda i, cc_ref: (0, i, 0))],
            out_specs=pl.BlockSpec((tr, B), lambda i, cc_ref: (cc_ref[1] * (h // tr) + i, 0))),
        out_shape=jax.ShapeDtypeStruct((2 * h, B), F32),
        compiler_params=_params(("parallel",)),
    )(chip_core, partial, recv)


def _sibling_assemble_call(shards, name="grad_sibling_assemble"):
    n = len(shards)

    def body(*refs):
        ins, outs = refs[:n], refs[n:2 * n]
        send_sems, recv_sems = refs[2 * n:]
        x, y, c = _position()
        copies = []
        for i in range(n):
            rows = shards[i].shape[0]
            cp = pltpu.make_async_remote_copy(src_ref=_half_rows(ins[i], c, rows), dst_ref=_half_rows(outs[i], c, rows),
                                              send_sem=send_sems.at[i], recv_sem=recv_sems.at[i],
                                              device_id=(x, y, 1 - c), device_id_type=MESH)
            cp.start()
            copies.append(cp)
        for cp in copies:
            cp.wait()

    return pl.pallas_call(
        body, name=name,
        in_specs=[ANY] * n, out_specs=[ANY] * n,
        out_shape=[jax.ShapeDtypeStruct(s.shape, F32) for s in shards],
        input_output_aliases={i: i for i in range(n)},
        scratch_shapes=[pltpu.SemaphoreType.DMA((n,)), pltpu.SemaphoreType.DMA((n,))],
    )(*shards)


N_DEVICES = 8


def _allsum_copies(srcs, lands):
    x, y, c = _position()
    me = 4 * x + 2 * y + c
    out = []
    for k in range(1, N_DEVICES):
        peer = (1 - x if k & 4 else x, 1 - y if k & 2 else y, 1 - c if k & 1 else c)
        out.append((srcs[0], lands[0].at[me], peer, k - 1))
    return out


def _ordered_sum_call(mine, landed, me):
    rows = mine.shape[0]

    def body(me_ref, x_ref, l_ref, o_ref):
        acc = jnp.where(me_ref[0] == 0, x_ref[...], l_ref[0])
        for d in range(1, N_DEVICES):
            acc = acc + jnp.where(me_ref[0] == d, x_ref[...], l_ref[d])
        o_ref[...] = acc

    return pl.pallas_call(
        body, name="small_grad_sum",
        in_specs=[pl.BlockSpec(memory_space=pltpu.SMEM), pl.BlockSpec(memory_space=pltpu.VMEM), pl.BlockSpec(memory_space=pltpu.VMEM)],
        out_specs=pl.BlockSpec(memory_space=pltpu.VMEM),
        out_shape=jax.ShapeDtypeStruct((rows, LANES), F32),
    )(me, mine, landed)


def _pack(arrays):
    flat = jnp.concatenate([a.reshape(-1).astype(F32) for a in arrays])
    rows = -(-flat.shape[0] // LANES)
    rows = -(-rows // SUBLANES) * SUBLANES
    flat = jnp.pad(flat, (0, rows * LANES - flat.shape[0]))
    return flat.reshape(rows, LANES)


def _unpack(packed, shapes):
    flat = packed.reshape(-1)
    out, off = [], 0
    for shp in shapes:
        size = int(np.prod(shp))
        out.append(flat[off:off + size].reshape(shp))
        off += size
    return out


def _local_step(xs, target, P, late_weights, on_grad):
    S, D = xs.shape
    qkv_width = 3 * N_HEADS * HEAD_DIM
    glu_col0, gate_col0 = qkv_width, qkv_width + 2 * D
    shard_major = lambda g: g.reshape(N_CHIPS, g.shape[0] // N_CHIPS, g.shape[1])

    h1 = _rms_fwd_call(xs, P["norm_mix_pre"])
    buckets = _bucket_tables()
    bias = _bias_table_call(P["rel_bias"] + 0.0 * h1[0, 0].astype(F32), buckets)
    P = dict(P, **late_weights("in", bias))
    proj = _matmul(h1, P["w_in"], "nn", "proj_in")
    parts = []
    for g in range(N_GROUPS):
        parts += _attn_fwd_call(proj, bias, g)
    a, a_bf, lse = _attn_merge_call(parts)
    P = dict(P, **late_weights("mix", a_bf))
    y_a = _matmul(a_bf, P["w_attn_out"], "nn", "attn_out")
    c1 = _conv_fwd_call(proj, glu_col0, P["conv_dw_w"], P["conv_dw_b"])
    cact = _ln_silu_call(c1, P["conv_ln_g"], P["conv_ln_b"])
    y_c = _matmul(cact, P["conv_pw_w"], "nn", "conv_pw")
    mixed = _mix_call(proj, gate_col0, P["b_gate"], y_a, y_c)
    out = _matmul(mixed, P["w_out"], "nn", "mix_out")
    x1, h2 = _res1_call(xs, out, P["norm_mix_post"], P["norm_ffn_pre"])
    P = dict(P, **late_weights("up", h2))
    u = _matmul(h2, P["w_up"], "nn", "ffn_up")
    f = _ffn_fwd_call(u, P["ffn_conv_w"], P["ffn_conv_b"])
    P = dict(P, **late_weights("down", f))
    yff = _matmul(f, P["w_down"], "nn", "ffn_down")
    loss_tile, dx2, dyff, dg_ffn_post = _loss_call(yff, x1, P["norm_ffn_post"], target)

    G = {}
    G["norm_ffn_post"] = dg_ffn_post
    zero = on_grad("w_down", shard_major(_matmul(f, dyff, "tn", "ffn_down_dw")))
    df = _matmul(dyff, P["w_down"], "nt", "ffn_down_dx")
    du, dwg, dwv, dbg, dbv = _ffn_bwd_call(u, P["ffn_conv_w"], P["ffn_conv_b"] + zero, df)
    G["ffn_conv_w"] = jnp.concatenate([dwg[:FFN_CONV_WIDTH], dwv[:FFN_CONV_WIDTH]], axis=1)
    G["ffn_conv_b"] = jnp.concatenate([dbg, dbv], axis=1)
    zero = on_grad("w_up", _matmul(h2, du, "tn", "ffn_up_dw", out_shards=True))
    dh2 = _matmul(du, P["w_up"], "nt", "ffn_up_dx")
    dx1, dout, G["norm_ffn_pre"], G["norm_mix_post"] = _mid_bwd_call(x1, P["norm_ffn_pre"] + zero, dh2, dx2, out, P["norm_mix_post"])
    zero = on_grad("w_out", shard_major(_matmul(mixed, dout, "tn", "mix_out_dw")))
    dmixed = _matmul(dout, P["w_out"], "nt", "mix_out_dx")
    dya, dyc, dga, dgc, dba, dbc = _mix_bwd_call(dmixed, proj, gate_col0, P["b_gate"] + zero, y_a, y_c)
    G["b_gate"] = jnp.concatenate([dba, dbc], axis=1)
    zero = on_grad("w_attn_out", _matmul(a_bf, dya, "tn", "attn_out_dw", out_shards=True))
    zero = zero + on_grad("conv_pw_w", shard_major(_matmul(cact, dyc, "tn", "conv_pw_dw")))
    da = _matmul(dya, P["w_attn_out"], "nt", "attn_out_dx")
    dcact = _matmul(dyc, P["conv_pw_w"], "nt", "conv_pw_dx")
    dc1, G["conv_ln_g"], G["conv_ln_b"] = _ln_silu_bwd_call(c1, P["conv_ln_g"] + zero, P["conv_ln_b"], dcact)
    dval, dgate, dw_dw, G["conv_dw_b"] = _conv_bwd_call(proj, glu_col0, P["conv_dw_w"], dc1)
    G["conv_dw_w"] = dw_dw[:CONV_WIDTH]
    delta = _attn_delta_call(a, da)
    dqs, dks, dvs, dbs = [], [], [], []
    for g in range(N_GROUPS):
        dq, dk, dv, db = _attn_bwd_call(proj, bias, da, lse, delta, g)
        dqs.append(dq)
        dks.append(dk)
        dvs.append(dv)
        dbs.append(db)
    G["rel_bias"] = _bias_grad_call(jnp.concatenate(dbs, axis=0), buckets)
    dproj = _dproj_call(dqs + dks + dvs, [dval, dgate, dga, dgc])
    zero, dproj = on_grad("w_in", _matmul(h1, dproj, "tn", "proj_in_dw", out_shards=True), carry=dproj)
    dh1 = _matmul(dproj, P["w_in"], "nt", "proj_in_dx")
    zero = zero + on_grad(None, dh1)
    grad_x, G["norm_mix_pre"] = _in_bwd_call(xs, P["norm_mix_pre"] + zero, dh1, dx1)
    return loss_tile, grad_x, G


def kernel(x, w_in, b_gate, rel_bias, w_attn_out, conv_dw_w, conv_dw_b, conv_ln_g, conv_ln_b, conv_pw_w, w_out, norm_mix_pre, norm_mix_post, norm_ffn_pre, norm_ffn_post, w_up, ffn_conv_w, ffn_conv_b, w_down, loss_target, m_w_in, m_b_gate, m_rel_bias, m_w_attn_out, m_conv_dw_w, m_conv_dw_b, m_conv_ln_g, m_conv_ln_b, m_conv_pw_w, m_w_out, m_norm_mix_pre, m_norm_mix_post, m_norm_ffn_pre, m_norm_ffn_post, m_w_up, m_ffn_conv_w, m_ffn_conv_b, m_w_down, v_w_in, v_b_gate, v_rel_bias, v_w_attn_out, v_conv_dw_w, v_conv_dw_b, v_conv_ln_g, v_conv_ln_b, v_conv_pw_w, v_w_out, v_norm_mix_pre, v_norm_mix_post, v_norm_ffn_pre, v_norm_ffn_post, v_w_up, v_ffn_conv_w, v_ffn_conv_b, v_w_down):
    weights = dict(w_in=w_in, b_gate=b_gate, rel_bias=rel_bias, w_attn_out=w_attn_out, conv_dw_w=conv_dw_w, conv_dw_b=conv_dw_b,
                   conv_ln_g=conv_ln_g, conv_ln_b=conv_ln_b, conv_pw_w=conv_pw_w, w_out=w_out, norm_mix_pre=norm_mix_pre,
                   norm_mix_post=norm_mix_post, norm_ffn_pre=norm_ffn_pre, norm_ffn_post=norm_ffn_post, w_up=w_up,
                   ffn_conv_w=ffn_conv_w, ffn_conv_b=ffn_conv_b, w_down=w_down)
    m_in = dict(w_in=m_w_in, b_gate=m_b_gate, rel_bias=m_rel_bias, w_attn_out=m_w_attn_out, conv_dw_w=m_conv_dw_w,
                conv_dw_b=m_conv_dw_b, conv_ln_g=m_conv_ln_g, conv_ln_b=m_conv_ln_b, conv_pw_w=m_conv_pw_w, w_out=m_w_out,
                norm_mix_pre=m_norm_mix_pre, norm_mix_post=m_norm_mix_post, norm_ffn_pre=m_norm_ffn_pre,
                norm_ffn_post=m_norm_ffn_post, w_up=m_w_up, ffn_conv_w=m_ffn_conv_w, ffn_conv_b=m_ffn_conv_b, w_down=m_w_down)
    v_in = dict(w_in=v_w_in, b_gate=v_b_gate, rel_bias=v_rel_bias, w_attn_out=v_w_attn_out, conv_dw_w=v_conv_dw_w,
                conv_dw_b=v_conv_dw_b, conv_ln_g=v_conv_ln_g, conv_ln_b=v_conv_ln_b, conv_pw_w=v_conv_pw_w, w_out=v_w_out,
                norm_mix_pre=v_norm_mix_pre, norm_mix_post=v_norm_mix_post, norm_ffn_pre=v_norm_ffn_pre,
                norm_ffn_post=v_norm_ffn_post, w_up=v_w_up, ffn_conv_w=v_ffn_conv_w, ffn_conv_b=v_ffn_conv_b, w_down=v_w_down)
    names = list(weights)
    xi, yi, ci = _position()
    chip = 2 * xi + yi
    core_arr = jnp.reshape(ci, (1,)).astype(jnp.int32)

    xs = x[0]
    target = loss_target[0]
    S, D = xs.shape

    big = ["w_in", "w_attn_out", "conv_pw_w", "w_out", "w_up", "w_down"]
    row_sharded = ("conv_pw_w", "w_out", "w_down")
    bf16_shard = {k: weights[k][0].astype(BF16) for k in big}
    natural = lambda k, g: g.reshape(-1, g.shape[2]) if k in row_sharded else g
    first_srcs = [bf16_shard["w_in"], conv_dw_w[0], ffn_conv_w[0]]
    first_lands = [lax.empty((N_CHIPS,) + s.shape, s.dtype) for s in first_srcs]
    (first_hop,) = _split_start("gather_in_start", [(first_srcs, first_lands, 4 * len(first_srcs), _first_hop_copies)], core_arr)
    launched = first_hop["token"]
    late_sets = dict(mix=["w_attn_out", "conv_pw_w", "w_out"], up=["w_up"], down=["w_down"])
    late_groups = []
    for keys in late_sets.values():
        srcs = [bf16_shard[k] for k in keys]
        late_groups.append((srcs, [lax.empty((N_CHIPS,) + s.shape, BF16) for s in srcs], 4 * len(keys), _gather_copies))
    started = {}

    def late_weights(tag, after):
        if tag == "in":
            w_in_halves, dw4, fc4 = _split_wait("gather_in_wait", first_hop, _first_hop_copies, after)[len(first_srcs):]
            (second_hop,) = _split_start("gather_in_pass_start", [([], [w_in_halves], 3, _second_hop_copies)], dw4)
            (w_in_full,) = _split_wait("gather_in_pass_wait", second_hop, _second_hop_copies, second_hop["tile"])
            started.update(zip(late_sets, _split_start("gather_late_start", late_groups, dw4, carry=w_in_full)))
            return dict(w_in=started["mix"]["carry"], conv_dw_w=jnp.concatenate(list(dw4), axis=1), ffn_conv_w=jnp.concatenate(list(fc4), axis=1))
        landed = _split_wait(f"gather_{tag}_wait", started[tag], _gather_copies, after)[len(late_sets[tag]):]
        return {k: natural(k, g) for k, g in zip(late_sets[tag], landed)}

    chip_core = jnp.stack([chip, ci]).astype(jnp.int32)
    exchanging, pending = {}, {}

    def launch(tag, g3, after, carry=None):
        keys, groups, partial = [], [], {}
        for k in list(exchanging):
            gk, r1 = _split_wait(f"sibling_exchange_wait_{k}", exchanging.pop(k), _sibling_copies, after)
            partial[k], s16 = _pair_sum_call(gk, r1, core_arr, f"pair_sum_{k}")
            keys.append(k)
            groups.append(([s16], [lax.empty((3,) + s16.shape[1:], BF16)], 3, _exchange_copies))
        if g3 is not None:
            groups.append(([g3], [lax.empty((N_CHIPS, g3.shape[1] // 2, g3.shape[2]), F32)], 1, _sibling_copies))
        begun = _split_start(f"grad_exchange_start_{tag}", groups, core_arr, carry)
        for k, st in zip(keys, begun):
            pending[k] = (partial[k], st)
        if g3 is not None:
            exchanging[tag] = begun[-1]
        return begun[0]["token"] if carry is None else (begun[0]["token"], begun[0]["carry"])

    def on_grad(k, g3, carry=None):
        if k is None:
            return launch("last", None, g3[:SUBLANES, :LANES])
        return launch(k, g3, g3[0, :SUBLANES, :LANES], carry)

    def finish(keys, after, tag):
        halves = []
        for k in keys:
            s32, st = pending[k]
            recv2 = _split_wait(f"chip_exchange_wait_{k}", st, _exchange_copies, after)[1]
            halves.append(_chip_sum_call(s32, recv2, chip_core, f"chip_sum_{k}"))
        return dict(zip(keys, _sibling_assemble_call(halves, f"grad_sibling_assemble_{tag}")))

    P = dict(b_gate=b_gate, rel_bias=rel_bias, conv_dw_b=conv_dw_b, conv_ln_g=conv_ln_g, conv_ln_b=conv_ln_b,
             norm_mix_pre=norm_mix_pre + launched, norm_mix_post=norm_mix_post, norm_ffn_pre=norm_ffn_pre,
             norm_ffn_post=norm_ffn_post, ffn_conv_b=ffn_conv_b)
    loss_tile, grad_x, G = _local_step(xs, target, P, late_weights, on_grad)

    small = [k for k in names if k not in big]
    packed = _pack([loss_tile[:1]] + [G[k] for k in small])
    (allsum,) = _split_start("small_grad_allsum_start",
                             [([packed], [jnp.zeros((N_DEVICES,) + packed.shape, F32)], N_DEVICES - 1, _allsum_copies)], core_arr)

    reduced, grads, deltas, new_m, new_v = {}, {}, {}, {}, {}

    def update(keys):
        for k in keys:
            gk, d, mn, vn = _adamw_call(weights[k][0], reduced[k], m_in[k][0], v_in[k][0], f"adamw_{k}")
            grads[k], deltas[k], new_m[k], new_v[k] = gk[None], d[None], mn[None], vn[None]

    others = [k for k in big if k != "w_in"]
    reduced.update(finish(others, allsum["tile"], "others"))
    update(others)
    reduced.update(finish(["w_in"], deltas["w_up"], "w_in"))
    update(["w_in"])

    me = jnp.reshape(4 * xi + 2 * yi + ci, (1,)).astype(jnp.int32)
    mine, landed = _split_wait("small_grad_allsum_wait", allsum, _allsum_copies, deltas["w_in"])
    summed_block = _ordered_sum_call(mine, landed, me)
    loss_row, *summed = _unpack(summed_block, [(1, LANES)] + [G[k].shape for k in small])
    loss = loss_row[0, 0]
    for k, gsum in zip(small, summed):
        if k in ("conv_dw_w", "ffn_conv_w"):
            cols = weights[k].shape[2]
            reduced[k] = lax.dynamic_slice_in_dim(gsum, chip * cols, cols, axis=1)
        else:
            reduced[k] = gsum
    for k in small:
        grads[k] = reduced[k].reshape(weights[k].shape)
    ds, mns, vns = _adamw_small_call([weights[k] for k in small], [grads[k] for k in small],
                                     [m_in[k] for k in small], [v_in[k] for k in small])
    deltas.update(zip(small, ds))
    new_m.update(zip(small, mns))
    new_v.update(zip(small, vns))

    return (loss, grad_x[None], *[grads[k] for k in names], *[deltas[k] for k in names],
            *[new_m[k] for k in names], *[new_v[k] for k in names])
```

```python
import functools
import math

import jax
import jax.numpy as jnp
import numpy as np
from jax import lax
from jax.experimental import pallas as pl
from jax.experimental.pallas import tpu as pltpu

F32 = jnp.float32
BF16 = jnp.bfloat16
MESH = pl.DeviceIdType.MESH

HEAD_DIM = 128
HEADS_PER_GROUP = 4
DILATED_PATTERNS = ((128, 1), (512, 4), (2048, 16))
N_GROUPS = 3
N_HEADS = N_GROUPS * HEADS_PER_GROUP
SPAN = 128
GROUP_WIDTH = HEADS_PER_GROUP * HEAD_DIM
CONV_WIDTH = 31
FFN_CONV_WIDTH = 3
N_BUCKETS = 32
MAX_DISTANCE = 2048
RMS_EPS = 1e-6
LN_EPS = 1e-5
NEG_INF = -1e30
ADAM_LR = 0.001
ADAM_B1 = 0.9
ADAM_B2 = 0.999
ADAM_EPS = 1e-08
ADAM_WD = 0.01
ADAM_STEP = 10

LANES = 128
SUBLANES = 8
ROW_TILE = 512
GATE_ROWS, GATE_COLS = 512, 512
TIME_BLOCK = 128
CONV_PAD = 32
FFN_PAD = 8
VMEM_LIMIT = 56 << 20


def _params(sem=None, vmem=None):
    kw = {}
    if sem is not None:
        kw["dimension_semantics"] = sem
    if vmem is not None:
        kw["vmem_limit_bytes"] = vmem
    return pltpu.CompilerParams(**kw)


def _pick(n, cands):
    for c in cands:
        if n % c == 0:
            return c
    return n


ELEMENTWISE_TILE_BYTES = 3 << 19


def _row_tile(rows, cols):
    for align in (16, SUBLANES):
        fits = [t for t in range(align, rows + 1, align) if rows % t == 0 and t * cols * 4 <= ELEMENTWISE_TILE_BYTES]
        if fits:
            return max(fits)
    return SUBLANES


N_CHIPS = 4
M_TILES = (1024, 1408, 512, 256, 128)
N_TILES = (1024, 512, 1408, 256, 128)
K_TILES = (2176, 2048, 1408, 1024, 512, 256, 128)


def _matmul(a, b, mode, name, out_shards=False, tm=None):
    assert a.dtype == BF16 and b.dtype == BF16, (name, a.dtype, b.dtype)
    b3 = b.ndim == 3
    tn = tk = None
    halves = None
    if mode == "nn":
        M, K = a.shape
        N = b.shape[-1] * (N_CHIPS if b3 else 1)
        tn = b.shape[-1] if b3 else None
    elif mode == "nt":
        if a.ndim == 3:
            halves = a.shape[2]
        M, K = a.shape[-2], a.shape[-1] * (a.shape[0] if a.ndim == 3 else 1)
        N = b.shape[-2]
        tk = b.shape[-1] if b3 else None
    else:
        if b3:
            halves = b.shape[2]
        K, M = a.shape
        N = b.shape[-1] * (b.shape[0] if b3 else 1)
        tn = N // N_CHIPS if out_shards else None
    tm = tm or _pick(M, M_TILES)
    tn = tn or _pick(N, N_TILES)
    tk = tk or _pick(K, K_TILES)
    nk = K // tk
    dn = {"nn": (((1,), (0,)), ((), ())), "nt": (((1,), (1,)), ((), ())), "tn": (((0,), (0,)), ((), ()))}[mode]

    def body(a_ref, b_ref, o_ref):
        if nk == 1:
            o_ref[...] = lax.dot_general(a_ref[...], b_ref[...], dn, preferred_element_type=F32)
        else:
            @pl.when(pl.program_id(2) == 0)
            def _():
                o_ref[...] = jnp.zeros_like(o_ref)

            o_ref[...] += lax.dot_general(a_ref[...], b_ref[...], dn, preferred_element_type=F32)

    if mode == "tn":
        a_spec = pl.BlockSpec((tk, tm), lambda i, j, k: (k, i))
    elif halves:
        per = halves // tk
        a_spec = pl.BlockSpec((None, tm, tk), lambda i, j, k: (k // per, i, k % per))
    else:
        a_spec = pl.BlockSpec((tm, tk), lambda i, j, k: (i, k))
    if mode == "nn":
        b_spec = pl.BlockSpec((None, tk, tn), lambda i, j, k: (j, k, 0)) if b3 else pl.BlockSpec((tk, tn), lambda i, j, k: (k, j))
    elif mode == "nt":
        b_spec = pl.BlockSpec((None, tn, tk), lambda i, j, k: (k, j, 0)) if b3 else pl.BlockSpec((tn, tk), lambda i, j, k: (j, k))
    elif halves:
        per = halves // tn
        b_spec = pl.BlockSpec((None, tk, tn), lambda i, j, k: (j // per, k, j % per))
    else:
        b_spec = pl.BlockSpec((tk, tn), lambda i, j, k: (k, j))
    if out_shards:
        out_spec = pl.BlockSpec((None, tm, tn), lambda i, j, k: (j, i, 0))
        out_shape = jax.ShapeDtypeStruct((N_CHIPS, M, tn), F32)
    else:
        out_spec = pl.BlockSpec((tm, tn), lambda i, j, k: (i, j))
        out_shape = jax.ShapeDtypeStruct((M, N), F32)
    return pl.pallas_call(
        body, name=name, grid=(M // tm, N // tn, nk),
        in_specs=[a_spec, b_spec], out_specs=out_spec, out_shape=out_shape,
        compiler_params=_params(("parallel", "parallel", "arbitrary"), VMEM_LIMIT),
    )(a, b)


def _rms(x, g):
    r = lax.rsqrt(jnp.mean(x * x, axis=-1, keepdims=True) + RMS_EPS)
    return x * r * g


def _rms_bwd(x, g, dy):
    r = lax.rsqrt(jnp.mean(x * x, axis=-1, keepdims=True) + RMS_EPS)
    n = x * r
    dn = dy * g
    dx = r * (dn - n * jnp.mean(dn * n, axis=-1, keepdims=True))
    return dx, jnp.sum(dy * n, axis=0, keepdims=True)


def _sigmoid(x):
    return 1.0 / (1.0 + jnp.exp(-x))


_GELU_C = math.sqrt(2.0 / math.pi)


def _gelu(x):
    return 0.5 * x * (1.0 + jnp.tanh(_GELU_C * (x + 0.044715 * x * x * x)))


def _gelu_and_grad(x):
    x2 = x * x
    t = jnp.tanh(_GELU_C * x * (1.0 + 0.044715 * x2))
    half = 0.5 * (1.0 + t)
    return x * half, half + (0.5 * _GELU_C) * x * (1.0 - t * t) * (1.0 + (3.0 * 0.044715) * x2)


def _row_spec(width, col_block=0):
    return pl.BlockSpec((ROW_TILE, width), lambda i: (i, col_block))


def _vec_spec(width, col_block=0):
    return pl.BlockSpec((1, width), lambda i: (0, col_block))


def _accumulate(ref, part):
    @pl.when(pl.program_id(0) == 0)
    def _():
        ref[...] = part

    @pl.when(pl.program_id(0) > 0)
    def _():
        ref[...] += part


def _rms_fwd_call(x, g):
    S, D = x.shape

    def body(x_ref, g_ref, h_ref):
        h_ref[...] = _rms(x_ref[...], g_ref[...]).astype(BF16)

    return pl.pallas_call(
        body, name="rms_mix_pre", grid=(S // ROW_TILE,),
        in_specs=[_row_spec(D), _vec_spec(D)], out_specs=_row_spec(D),
        out_shape=jax.ShapeDtypeStruct((S, D), BF16),
        compiler_params=_params(("parallel",)),
    )(x, g)


def _ln_silu_call(c1, g, b):
    S, C = c1.shape

    def body(c_ref, g_ref, b_ref, o_ref):
        xv = c_ref[...]
        mu = jnp.mean(xv, axis=-1, keepdims=True)
        xc = xv - mu
        var = jnp.mean(xc * xc, axis=-1, keepdims=True)
        z = xc * lax.rsqrt(var + LN_EPS) * g_ref[...] + b_ref[...]
        o_ref[...] = (z * _sigmoid(z)).astype(BF16)

    return pl.pallas_call(
        body, name="conv_ln_silu", grid=(S // ROW_TILE,),
        in_specs=[_row_spec(C), _vec_spec(C), _vec_spec(C)], out_specs=_row_spec(C),
        out_shape=jax.ShapeDtypeStruct((S, C), BF16),
        compiler_params=_params(("parallel",)),
    )(c1, g, b)


def _ln_silu_bwd_call(c1, g, b, dc):
    S, C = c1.shape

    def body(c_ref, g_ref, b_ref, dc_ref, dx_ref, dg_ref, db_ref):
        xv = c_ref[...]
        mu = jnp.mean(xv, axis=-1, keepdims=True)
        xc = xv - mu
        rs = lax.rsqrt(jnp.mean(xc * xc, axis=-1, keepdims=True) + LN_EPS)
        xh = xc * rs
        z = xh * g_ref[...] + b_ref[...]
        sg = _sigmoid(z)
        dz = dc_ref[...] * (sg * (1.0 + z * (1.0 - sg)))
        dxh = dz * g_ref[...]
        dx_ref[...] = rs * (dxh - jnp.mean(dxh, axis=-1, keepdims=True) - xh * jnp.mean(dxh * xh, axis=-1, keepdims=True))
        _accumulate(dg_ref, jnp.sum(dz * xh, axis=0, keepdims=True))
        _accumulate(db_ref, jnp.sum(dz, axis=0, keepdims=True))

    return pl.pallas_call(
        body, name="conv_ln_silu_bwd", grid=(S // ROW_TILE,),
        in_specs=[_row_spec(C), _vec_spec(C), _vec_spec(C), _row_spec(C)],
        out_specs=[_row_spec(C), _vec_spec(C), _vec_spec(C)],
        out_shape=[jax.ShapeDtypeStruct((S, C), F32), jax.ShapeDtypeStruct((1, C), F32), jax.ShapeDtypeStruct((1, C), F32)],
        compiler_params=_params(("arbitrary",)),
    )(c1, g, b, dc)


def _mix_call(proj, gate_col0, b_gate, y_a, y_c):
    S, D = y_a.shape
    w = GATE_COLS
    nc = D // w
    ga0, gc0 = gate_col0 // w, (gate_col0 + D) // w

    def body(ga_ref, gc_ref, ba_ref, bc_ref, ya_ref, yc_ref, o_ref):
        o_ref[...] = (_sigmoid(ga_ref[...] + ba_ref[...]) * ya_ref[...]
                      + _sigmoid(gc_ref[...] + bc_ref[...]) * yc_ref[...]).astype(BF16)

    tile = lambda off: pl.BlockSpec((GATE_ROWS, w), lambda i, j: (i, off + j))
    vec = lambda off: pl.BlockSpec((1, w), lambda i, j: (0, off + j))
    return pl.pallas_call(
        body, name="gate_mix", grid=(S // GATE_ROWS, nc),
        in_specs=[tile(ga0), tile(gc0), vec(0), vec(nc), tile(0), tile(0)],
        out_specs=tile(0), out_shape=jax.ShapeDtypeStruct((S, D), BF16),
        compiler_params=_params(("parallel", "parallel")),
    )(proj, proj, b_gate, b_gate, y_a, y_c)


def _mix_bwd_call(dmixed, proj, gate_col0, b_gate, y_a, y_c):
    S, D = y_a.shape
    w = GATE_COLS
    nc = D // w
    ga0, gc0 = gate_col0 // w, (gate_col0 + D) // w

    def body(dm_ref, ga_ref, gc_ref, ba_ref, bc_ref, ya_ref, yc_ref, dya_ref, dyc_ref, dga_ref, dgc_ref, dba_ref, dbc_ref):
        dm = dm_ref[...]
        sa = _sigmoid(ga_ref[...] + ba_ref[...])
        sc = _sigmoid(gc_ref[...] + bc_ref[...])
        dya_ref[...] = (dm * sa).astype(BF16)
        dyc_ref[...] = (dm * sc).astype(BF16)
        dga = dm * ya_ref[...] * sa * (1.0 - sa)
        dgc = dm * yc_ref[...] * sc * (1.0 - sc)
        dga_ref[...] = dga.astype(BF16)
        dgc_ref[...] = dgc.astype(BF16)
        pa = jnp.sum(dga, axis=0, keepdims=True)
        pc = jnp.sum(dgc, axis=0, keepdims=True)

        @pl.when(pl.program_id(1) == 0)
        def _():
            dba_ref[...] = pa
            dbc_ref[...] = pc

        @pl.when(pl.program_id(1) > 0)
        def _():
            dba_ref[...] += pa
            dbc_ref[...] += pc

    tile = lambda off: pl.BlockSpec((GATE_ROWS, w), lambda j, i: (i, off + j))
    vec = lambda off: pl.BlockSpec((1, w), lambda j, i: (0, off + j))
    return pl.pallas_call(
        body, name="gate_mix_bwd", grid=(nc, S // GATE_ROWS),
        in_specs=[tile(0), tile(ga0), tile(gc0), vec(0), vec(nc), tile(0), tile(0)],
        out_specs=[tile(0), tile(0), tile(0), tile(0), vec(0), vec(0)],
        out_shape=[jax.ShapeDtypeStruct((S, D), BF16)] * 4 + [
                   jax.ShapeDtypeStruct((1, D), F32), jax.ShapeDtypeStruct((1, D), F32)],
        compiler_params=_params(("parallel", "arbitrary")),
    )(dmixed, proj, proj, b_gate, b_gate, y_a, y_c)


def _res1_call(x, out, g_post, g_pre):
    S, D = x.shape

    def body(x_ref, o_ref, gp_ref, gq_ref, x1_ref, h2_ref):
        x1 = x_ref[...] + _rms(o_ref[...], gp_ref[...])
        x1_ref[...] = x1
        h2_ref[...] = _rms(x1, gq_ref[...]).astype(BF16)

    return pl.pallas_call(
        body, name="residual_mix", grid=(S // ROW_TILE,),
        in_specs=[_row_spec(D), _row_spec(D), _vec_spec(D), _vec_spec(D)],
        out_specs=[_row_spec(D), _row_spec(D)],
        out_shape=[jax.ShapeDtypeStruct((S, D), F32), jax.ShapeDtypeStruct((S, D), BF16)],
        compiler_params=_params(("parallel",)),
    )(x, out, g_post, g_pre)


def _loss_call(y, x1, g_post, target):
    S, D = y.shape

    def body(y_ref, x1_ref, g_ref, t_ref, loss_ref, dx_ref, dy_ref, dg_ref):
        yv, gv = y_ref[...], g_ref[...]
        err = x1_ref[...] + _rms(yv, gv) - t_ref[...]
        dx2 = err * (1.0 / D)
        dx_ref[...] = dx2
        dy, dg = _rms_bwd(yv, gv, dx2)
        dy_ref[...] = dy.astype(BF16)
        _accumulate(dg_ref, dg)
        part = 0.5 * jnp.sum(jnp.mean(err * err, axis=-1, keepdims=True), axis=0, keepdims=True)
        _accumulate(loss_ref, jnp.broadcast_to(part, (SUBLANES, LANES)))

    return pl.pallas_call(
        body, name="residual_ffn_loss", grid=(S // ROW_TILE,),
        in_specs=[_row_spec(D), _row_spec(D), _vec_spec(D), _row_spec(D)],
        out_specs=[pl.BlockSpec((SUBLANES, LANES), lambda i: (0, 0)), _row_spec(D), _row_spec(D), _vec_spec(D)],
        out_shape=[jax.ShapeDtypeStruct((SUBLANES, LANES), F32), jax.ShapeDtypeStruct((S, D), F32),
                   jax.ShapeDtypeStruct((S, D), BF16), jax.ShapeDtypeStruct((1, D), F32)],
        compiler_params=_params(("arbitrary",)),
    )(y, x1, g_post, target)


def _mid_bwd_call(x1, g_pre, dh2, dx2, out, g_post):
    S, D = x1.shape

    def body(x1_ref, gq_ref, dh_ref, dx2_ref, o_ref, gp_ref, dx1_ref, do_ref, dgq_ref, dgp_ref):
        d, dgq = _rms_bwd(x1_ref[...], gq_ref[...], dh_ref[...])
        dx1 = dx2_ref[...] + d
        dx1_ref[...] = dx1
        do, dgp = _rms_bwd(o_ref[...], gp_ref[...], dx1)
        do_ref[...] = do.astype(BF16)
        _accumulate(dgq_ref, dgq)
        _accumulate(dgp_ref, dgp)

    return pl.pallas_call(
        body, name="residual_mix_bwd", grid=(S // ROW_TILE,),
        in_specs=[_row_spec(D), _vec_spec(D), _row_spec(D), _row_spec(D), _row_spec(D), _vec_spec(D)],
        out_specs=[_row_spec(D), _row_spec(D), _vec_spec(D), _vec_spec(D)],
        out_shape=[jax.ShapeDtypeStruct((S, D), F32), jax.ShapeDtypeStruct((S, D), BF16)] + [jax.ShapeDtypeStruct((1, D), F32)] * 2,
        compiler_params=_params(("arbitrary",)),
    )(x1, g_pre, dh2, dx2, out, g_post)


def _in_bwd_call(x, g, dh1, dx1):
    S, D = x.shape

    def body(x_ref, g_ref, dh_ref, dx1_ref, gx_ref, dg_ref):
        d, dg = _rms_bwd(x_ref[...], g_ref[...], dh_ref[...])
        gx_ref[...] = dx1_ref[...] + d
        _accumulate(dg_ref, dg)

    return pl.pallas_call(
        body, name="rms_mix_pre_bwd", grid=(S // ROW_TILE,),
        in_specs=[_row_spec(D), _vec_spec(D), _row_spec(D), _row_spec(D)],
        out_specs=[_row_spec(D), _vec_spec(D)],
        out_shape=[jax.ShapeDtypeStruct((S, D), F32), jax.ShapeDtypeStruct((1, D), F32)],
        compiler_params=_params(("arbitrary",)),
    )(x, g, dh1, dx1)


def _bucket_table(dilation):
    qi = np.arange(SPAN)[:, None]
    ki = np.arange(2 * SPAN)[None, :]
    dist = np.maximum(qi + SPAN - ki, 0) * dilation
    max_exact = N_BUCKETS // 2
    d = np.maximum(dist, 1).astype(np.float64)
    large = max_exact + (np.log(d / max_exact) / math.log(MAX_DISTANCE / max_exact) * (N_BUCKETS - max_exact)).astype(np.int32)
    large = np.minimum(large, N_BUCKETS - 1)
    return np.where(dist < max_exact, dist, large).astype(np.int32)


def _bucket_tables():
    return jnp.asarray(np.stack([_bucket_table(r) for _, r in DILATED_PATTERNS]))


def _bias_table_call(rel_bias, buckets):
    def body(rb_ref, bk_ref, o_ref):
        for h in range(N_HEADS):
            bk = bk_ref[h // HEADS_PER_GROUP]

            def step(b, acc):
                return jnp.where(bk == b, rb_ref[b, h], acc)

            o_ref[h] = lax.fori_loop(0, N_BUCKETS, step, jnp.zeros((SPAN, 2 * SPAN), F32))

    return pl.pallas_call(
        body, name="rel_bias_table",
        in_specs=[pl.BlockSpec(memory_space=pltpu.SMEM), pl.BlockSpec(memory_space=pltpu.VMEM)],
        out_specs=pl.BlockSpec(memory_space=pltpu.VMEM),
        out_shape=jax.ShapeDtypeStruct((N_HEADS, SPAN, 2 * SPAN), F32),
    )(rel_bias, buckets)


def _bias_grad_call(dbias, buckets):
    def body(db_ref, bk_ref, o_ref, rows_ref):
        for h in range(N_HEADS):
            bk = bk_ref[h // HEADS_PER_GROUP]
            dv = db_ref[h]

            def step(b, carry):
                rows_ref[h, b] = jnp.sum(jnp.where(bk == b, dv, 0.0), axis=0, keepdims=True)
                return carry

            lax.fori_loop(0, N_BUCKETS, step, 0)
        o_ref[...] = jnp.sum(rows_ref[...], axis=-1, keepdims=True)

    out = pl.pallas_call(
        body, name="rel_bias_grad",
        in_specs=[pl.BlockSpec(memory_space=pltpu.VMEM), pl.BlockSpec(memory_space=pltpu.VMEM)],
        out_specs=pl.BlockSpec(memory_space=pltpu.VMEM),
        out_shape=jax.ShapeDtypeStruct((N_HEADS, N_BUCKETS, 1, 1), F32),
        scratch_shapes=[pltpu.VMEM((N_HEADS, N_BUCKETS, 1, 2 * SPAN), F32)],
    )(dbias, buckets)
    return out.reshape(N_HEADS, N_BUCKETS).T


def _dot_nt(a, b):
    return lax.dot_general(a, b, (((1,), (1,)), ((), ())), preferred_element_type=F32)


def _dot_nn(a, b):
    return lax.dot_general(a, b, (((1,), (0,)), ((), ())), preferred_element_type=F32)


def _dot_tn(a, b):
    return lax.dot_general(a, b, (((0,), (0,)), ((), ())), preferred_element_type=F32)


def _band_masks(n, nb):
    qi = lax.broadcasted_iota(jnp.int32, (SPAN, SPAN), 0)
    ki = lax.broadcasted_iota(jnp.int32, (SPAN, SPAN), 1)
    prev_ok = jnp.logical_and(ki >= qi, n > 0)
    cur_ok = ki <= qi
    next_ok = jnp.logical_and(ki >= qi, n < nb - 1)
    return prev_ok, cur_ok, next_ok


def _wide_band_mask(n):
    qi = lax.broadcasted_iota(jnp.int32, (SPAN, 2 * SPAN), 0)
    ki = lax.broadcasted_iota(jnp.int32, (SPAN, 2 * SPAN), 1)
    prev_ok = jnp.logical_and(jnp.logical_and(ki < SPAN, ki >= qi), n > 0)
    cur_ok = jnp.logical_and(ki >= SPAN, ki - SPAN <= qi)
    return jnp.logical_or(prev_ok, cur_ok)


def _attn_plan(S, group):
    r = DILATED_PATTERNS[group][1]
    hp, per = (HEADS_PER_GROUP, 1) if r == 1 else (2, 4)
    return r, S // (r * SPAN), hp, per


def _residue_rows(rho, r):
    return slice(None) if r == 1 else pl.ds(rho, SPAN, stride=r)


def _for_residues(r, per, fn):
    if r == per:
        for u in range(per):
            fn(u)
        return

    def step(i, carry):
        for u in range(per):
            fn(i * per + u)
        return carry

    lax.fori_loop(0, r // per, step, 0)


def _attn_fwd_call(proj, bias, group):
    S = proj.shape[0]
    r, nb, hp, per = _attn_plan(S, group)
    scale = HEAD_DIM ** -0.5
    kinds = ("q", "kp", "kc", "vp", "vc") if nb > 1 else ("q", "kc", "vc")

    per_kind = _refs_per_kind(r, hp)

    def body(*refs):
        ins = {kind: refs[i * per_kind:(i + 1) * per_kind] for i, kind in enumerate(kinds)}
        b_ref, o_ref, lse_ref = refs[len(kinds) * per_kind:]
        n = pl.program_id(1)
        prev_ok, cur_ok, _ = _band_masks(n, nb)

        band_ok = _wide_band_mask(n) if nb > 1 else cur_ok

        def residue(rho):
            rows = _residue_rows(rho, r)
            for j in range(hp):
                get = lambda kind: _head_rows(ins[kind], j, rows, r).astype(BF16)
                q = get("q")
                if nb > 1:
                    keys, vals, bias_j = jnp.concatenate([get("kp"), get("kc")], axis=0), jnp.concatenate([get("vp"), get("vc")], axis=0), b_ref[j]
                else:
                    keys, vals, bias_j = get("kc"), get("vc"), b_ref[j, :, SPAN:]
                s = jnp.where(band_ok, _dot_nt(q, keys) * scale + bias_j, NEG_INF)
                m = jnp.max(s, axis=-1, keepdims=True)
                p = jnp.exp(s - m)
                den = jnp.sum(p, axis=-1, keepdims=True)
                o_ref[j, rows, :] = _dot_nn(p.astype(BF16), vals) / den
                lse_ref[j, rows, :] = jnp.broadcast_to(m + jnp.log(den), (SPAN, HEAD_DIM))

        _for_residues(r, per, residue)

    in_specs = [_head_spec(r, nb, hp, kind, group, jj) for kind in kinds for jj in range(per_kind)]
    in_specs.append(pl.BlockSpec((hp, SPAN, 2 * SPAN), lambda j, n: (group * (HEADS_PER_GROUP // hp) + j, 0, 0)))
    out = pl.BlockSpec((hp, r * SPAN, HEAD_DIM), lambda j, n: (j, n, 0))
    return pl.pallas_call(
        body, name=f"attn_fwd_g{group}", grid=(HEADS_PER_GROUP // hp, nb),
        in_specs=in_specs, out_specs=[out] * 2,
        out_shape=[jax.ShapeDtypeStruct((HEADS_PER_GROUP, S, HEAD_DIM), F32)] * 2,
        compiler_params=_params(("parallel", "parallel"), VMEM_LIMIT),
    )(*([proj] * (len(in_specs) - 1)), bias)


_PROJ_PART = dict(q=0, qn=0, kp=1, kc=1, vp=2, vc=2)


def _refs_per_kind(r, hp):
    return 1 if r == 1 else hp


def _head_rows(refs, j, rows, r):
    return refs[0][:, j * HEAD_DIM:(j + 1) * HEAD_DIM] if r == 1 else refs[j][rows, :]


def _head_spec(r, nb, hp, kind, group, jj):
    if kind in _PROJ_PART:
        base = (_PROJ_PART[kind] * N_GROUPS + group) * HEADS_PER_GROUP
    else:
        base = 0
    if kind.endswith("p"):
        row = lambda n: jnp.maximum(n - 1, 0)
    elif kind.endswith("n"):
        row = lambda n: jnp.minimum(n + 1, nb - 1)
    else:
        row = lambda n: n
    if r == 1:
        return pl.BlockSpec((SPAN, hp * HEAD_DIM), lambda j, n: (row(n), base // hp + j))
    return pl.BlockSpec((r * SPAN, HEAD_DIM), lambda j, n: (row(n), base + j * hp + jj))


def _attn_merge_call(parts):
    S = parts[0].shape[1]

    def body(o1, s1, o2, s2, o3, s3, a_ref, ab_ref, lse_ref):
        for j in range(HEADS_PER_GROUP):
            sl = slice(j * HEAD_DIM, (j + 1) * HEAD_DIM)
            mx = jnp.maximum(jnp.maximum(s1[j], s2[j]), s3[j])
            w1 = jnp.exp(s1[j] - mx)
            w2 = jnp.exp(s2[j] - mx)
            w3 = jnp.exp(s3[j] - mx)
            den = w1 + w2 + w3
            a = (w1 * o1[j] + w2 * o2[j] + w3 * o3[j]) / den
            a_ref[:, sl] = a
            ab_ref[:, sl] = a.astype(BF16)
            lse_ref[:, sl] = mx + jnp.log(den)

    heads = pl.BlockSpec((HEADS_PER_GROUP, ROW_TILE, HEAD_DIM), lambda i: (0, i, 0))
    return pl.pallas_call(
        body, name="attn_merge", grid=(S // ROW_TILE,),
        in_specs=[heads] * 6, out_specs=[_row_spec(GROUP_WIDTH)] * 3,
        out_shape=[jax.ShapeDtypeStruct((S, GROUP_WIDTH), F32), jax.ShapeDtypeStruct((S, GROUP_WIDTH), BF16),
                   jax.ShapeDtypeStruct((S, GROUP_WIDTH), F32)],
        compiler_params=_params(("parallel",)),
    )(*parts)


def _attn_delta_call(a, da):
    S = a.shape[0]

    def body(a_ref, da_ref, d_ref):
        for j in range(HEADS_PER_GROUP):
            sl = slice(j * HEAD_DIM, (j + 1) * HEAD_DIM)
            d = jnp.sum(a_ref[:, sl] * da_ref[:, sl], axis=-1, keepdims=True)
            d_ref[:, sl] = jnp.broadcast_to(d, (ROW_TILE, HEAD_DIM))

    return pl.pallas_call(
        body, name="attn_delta", grid=(S // ROW_TILE,),
        in_specs=[_row_spec(GROUP_WIDTH)] * 2, out_specs=_row_spec(GROUP_WIDTH),
        out_shape=jax.ShapeDtypeStruct((S, GROUP_WIDTH), F32),
        compiler_params=_params(("parallel",)),
    )(a, da)


def _attn_bwd_call(proj, bias, da, lse, delta, group):
    S = proj.shape[0]
    r, nb, hp, per = _attn_plan(S, group)
    scale = HEAD_DIM ** -0.5
    kinds = ("q", "qn", "kp", "kc", "vp", "vc", "da", "dan", "lse", "lsen", "dl", "dln") if nb > 1 else ("q", "kc", "vc", "da", "lse", "dl")
    source = dict(da=da, dan=da, lse=lse, lsen=lse, dl=delta, dln=delta)

    per_kind = _refs_per_kind(r, hp)

    def body(*refs):
        ins = {kind: refs[i * per_kind:(i + 1) * per_kind] for i, kind in enumerate(kinds)}
        b_ref, dq_ref, dk_ref, dv_ref, db_ref = refs[len(kinds) * per_kind:]
        n = pl.program_id(1)
        prev_ok, cur_ok, next_ok = _band_masks(n, nb)

        @pl.when(n == 0)
        def _():
            db_ref[...] = jnp.zeros_like(db_ref)

        band_ok = _wide_band_mask(n) if nb > 1 else cur_ok

        def residue(rho):
            rows = _residue_rows(rho, r)
            for j in range(hp):
                get = lambda kind: _head_rows(ins[kind], j, rows, r)
                q = get("q").astype(BF16)
                kc = get("kc").astype(BF16)
                vc = get("vc").astype(BF16)
                dav = get("da").astype(BF16)
                lse_q, dl_q = get("lse"), get("dl")
                if nb == 1:
                    pc = jnp.exp(jnp.where(cur_ok, _dot_nt(q, kc) * scale + b_ref[j, :, SPAN:], NEG_INF) - lse_q)
                    dsc = pc * (_dot_nt(dav, vc) - dl_q)
                    dsc_b = dsc.astype(BF16)
                    dq = _dot_nn(dsc_b, kc)
                    dk = _dot_tn(dsc_b, q)
                    dv = _dot_tn(pc.astype(BF16), dav)
                    db_ref[j, :, SPAN:] += dsc
                else:
                    qn = get("qn").astype(BF16)
                    dan = get("dan").astype(BF16)
                    keys = jnp.concatenate([get("kp").astype(BF16), kc], axis=0)
                    vals = jnp.concatenate([get("vp").astype(BF16), vc], axis=0)
                    wide = lambda t: jnp.concatenate([t, t], axis=1)
                    p = jnp.exp(jnp.where(band_ok, _dot_nt(q, keys) * scale + b_ref[j], NEG_INF) - wide(lse_q))
                    ds = p * (_dot_nt(dav, vals) - wide(dl_q))
                    dq = _dot_nn(ds.astype(BF16), keys)
                    db_ref[j] += ds
                    pn = jnp.exp(jnp.where(next_ok, _dot_nt(qn, kc) * scale + b_ref[j, :, :SPAN], NEG_INF) - get("lsen"))
                    dsn = pn * (_dot_nt(dan, vc) - get("dln"))
                    both = lambda cur_part, next_part: jnp.concatenate([cur_part.astype(BF16), next_part.astype(BF16)], axis=0)
                    dk = _dot_tn(both(ds[:, SPAN:], dsn), jnp.concatenate([q, qn], axis=0))
                    dv = _dot_tn(both(p[:, SPAN:], pn), jnp.concatenate([dav, dan], axis=0))
                dq_ref[j, rows, :] = dq * scale
                dk_ref[j, rows, :] = dk * scale
                dv_ref[j, rows, :] = dv

        _for_residues(r, per, residue)

    per_group = HEADS_PER_GROUP // hp
    band = (hp, SPAN, 2 * SPAN)
    in_specs = [_head_spec(r, nb, hp, kind, group, jj) for kind in kinds for jj in range(per_kind)]
    in_specs.append(pl.BlockSpec(band, lambda j, n: (group * per_group + j, 0, 0)))
    operands = [source.get(kind, proj) for kind in kinds for _ in range(per_kind)] + [bias]
    out = pl.BlockSpec((hp, r * SPAN, HEAD_DIM), lambda j, n: (j, n, 0))
    return pl.pallas_call(
        body, name=f"attn_bwd_g{group}", grid=(per_group, nb),
        in_specs=in_specs,
        out_specs=[out] * 3 + [pl.BlockSpec(band, lambda j, n: (j, 0, 0))],
        out_shape=[jax.ShapeDtypeStruct((HEADS_PER_GROUP, S, HEAD_DIM), F32)] * 3
        + [jax.ShapeDtypeStruct((HEADS_PER_GROUP, SPAN, 2 * SPAN), F32)],
        compiler_params=_params(("parallel", "arbitrary"), VMEM_LIMIT),
    )(*operands)


def _dproj_call(dqkv, tails):
    S = tails[0].shape[0]
    width = len(dqkv) * GROUP_WIDTH + sum(t.shape[1] for t in tails)

    def body(*refs):
        o_ref = refs[-1]
        col = 0
        for ref in refs[:len(dqkv)]:
            for j in range(HEADS_PER_GROUP):
                o_ref[:, col:col + HEAD_DIM] = ref[j].astype(BF16)
                col += HEAD_DIM
        for ref in refs[len(dqkv):-1]:
            o_ref[:, col:col + ref.shape[1]] = ref[...]
            col += ref.shape[1]

    heads = pl.BlockSpec((HEADS_PER_GROUP, ROW_TILE, HEAD_DIM), lambda i: (0, i, 0))
    return pl.pallas_call(
        body, name="dproj_assemble", grid=(S // ROW_TILE,),
        in_specs=[heads] * len(dqkv) + [_row_spec(t.shape[1]) for t in tails],
        out_specs=_row_spec(width), out_shape=jax.ShapeDtypeStruct((S, width), BF16),
        compiler_params=_params(("parallel",)),
    )(*dqkv, *tails)


def _tap_rows(xpad_ref, t0, k, width, pad):
    return xpad_ref[pl.ds(t0 + (pad - (width - 1 - k)), TIME_BLOCK), :]


def _conv_block(xpad_ref, t0, w_ref, width, pad):
    acc = None
    for k in range(width):
        term = w_ref[k:k + 1, :] * _tap_rows(xpad_ref, t0, k, width, pad)
        acc = term if acc is None else acc + term
    return acc


def _conv_transpose_block(dpad_ref, t0, w_ref, width):
    acc = None
    for k in range(width):
        term = w_ref[k:k + 1, :] * dpad_ref[pl.ds(t0 + (width - 1 - k), TIME_BLOCK), :]
        acc = term if acc is None else acc + term
    return acc


def _conv_weight_grad(xpad_ref, t0, dy, dw_ref, width, pad):
    for k in range(width):
        dw_ref[k:k + 1, :] += jnp.sum(dy * _tap_rows(xpad_ref, t0, k, width, pad), axis=0, keepdims=True)


def _time_loop(S, step, skip_first=0, skip_last=0):
    def it(tb, carry):
        step(pl.multiple_of(tb * TIME_BLOCK, TIME_BLOCK))
        return carry

    lax.fori_loop(skip_first, S // TIME_BLOCK - skip_last, it, 0)


def _fill_head(head_ref, x_ref, pad):
    head_ref[0:pad, :] = jnp.zeros((pad, LANES), F32)
    head_ref[pad:, :] = x_ref[0:TIME_BLOCK, :]


def _fill_tail(tail_ref, x_ref, pad):
    S = x_ref.shape[0]
    tail_ref[0:TIME_BLOCK, :] = x_ref[S - TIME_BLOCK:S, :]
    tail_ref[TIME_BLOCK:, :] = jnp.zeros((pad, LANES), F32)


def _conv_fwd_call(proj, col0, w, b):
    S = proj.shape[0]
    C = w.shape[1]
    nt = C // LANES
    v0, g0 = col0 // LANES, (col0 + C) // LANES

    def body(val_ref, gate_ref, w_ref, b_ref, o_ref, pad_ref):
        pad_ref[0:CONV_PAD, :] = jnp.zeros((CONV_PAD, LANES), F32)
        pad_ref[CONV_PAD:, :] = val_ref[...] * _sigmoid(gate_ref[...])

        def step(t0):
            o_ref[pl.ds(t0, TIME_BLOCK), :] = _conv_block(pad_ref, t0, w_ref, CONV_WIDTH, CONV_PAD) + b_ref[...]

        _time_loop(S, step)

    seq = lambda off: pl.BlockSpec((S, LANES), lambda i: (0, off + i))
    return pl.pallas_call(
        body, name="conv_module", grid=(nt,),
        in_specs=[seq(v0), seq(g0), pl.BlockSpec((CONV_WIDTH, LANES), lambda i: (0, i)), pl.BlockSpec((1, LANES), lambda i: (0, i))],
        out_specs=seq(0), out_shape=jax.ShapeDtypeStruct((S, C), F32),
        scratch_shapes=[pltpu.VMEM((S + CONV_PAD, LANES), F32)],
        compiler_params=_params(("parallel",)),
    )(proj, proj, w, b)


def _conv_bwd_call(proj, col0, w, dc1):
    S = proj.shape[0]
    C = w.shape[1]
    nt = C // LANES
    v0, g0 = col0 // LANES, (col0 + C) // LANES

    def body(val_ref, gate_ref, w_ref, dy_ref, dval_ref, dgate_ref, dw_ref, db_ref, xpad_ref, tail_ref, dwacc_ref):
        xpad_ref[0:CONV_PAD, :] = jnp.zeros((CONV_PAD, LANES), F32)
        xpad_ref[CONV_PAD:, :] = val_ref[...] * _sigmoid(gate_ref[...])
        _fill_tail(tail_ref, dy_ref, CONV_PAD)
        dwacc_ref[...] = jnp.zeros_like(dwacc_ref)

        def block(t0, dy_src, dy_t0):
            rows = pl.ds(t0, TIME_BLOCK)
            _conv_weight_grad(xpad_ref, t0, dy_ref[rows, :], dwacc_ref, CONV_WIDTH, CONV_PAD)
            dc0 = _conv_transpose_block(dy_src, dy_t0, w_ref, CONV_WIDTH)
            sg = _sigmoid(gate_ref[rows, :])
            dval_ref[rows, :] = (dc0 * sg).astype(BF16)
            dgate_ref[rows, :] = (dc0 * val_ref[rows, :] * sg * (1.0 - sg)).astype(BF16)

        _time_loop(S, lambda t0: block(t0, dy_ref, t0), skip_last=1)
        block(S - TIME_BLOCK, tail_ref, 0)
        dw_ref[...] = dwacc_ref[...]
        db_ref[...] = jnp.sum(dy_ref[...], axis=0, keepdims=True)

    seq = lambda off: pl.BlockSpec((S, LANES), lambda i: (0, off + i))
    return pl.pallas_call(
        body, name="conv_module_bwd", grid=(nt,),
        in_specs=[seq(v0), seq(g0), pl.BlockSpec((CONV_WIDTH, LANES), lambda i: (0, i)), seq(0)],
        out_specs=[seq(0), seq(0), pl.BlockSpec((CONV_PAD, LANES), lambda i: (0, i)), pl.BlockSpec((1, LANES), lambda i: (0, i))],
        out_shape=[jax.ShapeDtypeStruct((S, C), BF16), jax.ShapeDtypeStruct((S, C), BF16),
                   jax.ShapeDtypeStruct((CONV_PAD, C), F32), jax.ShapeDtypeStruct((1, C), F32)],
        scratch_shapes=[pltpu.VMEM((S + CONV_PAD, LANES), F32), pltpu.VMEM((TIME_BLOCK + CONV_PAD, LANES), F32),
                        pltpu.VMEM((CONV_PAD, LANES), F32)],
        compiler_params=_params(("parallel",)),
    )(proj, proj, w, dc1)


def _ffn_fwd_call(u, w, b):
    S, C2 = u.shape
    C = C2 // 2
    nt = C // LANES

    def body(ug_ref, uv_ref, wg_ref, wv_ref, bg_ref, bv_ref, f_ref, hg_ref, hv_ref):
        _fill_head(hg_ref, ug_ref, FFN_PAD)
        _fill_head(hv_ref, uv_ref, FFN_PAD)

        def block(t0, xg_ref, xv_ref, x_t0, pad):
            cg = _conv_block(xg_ref, x_t0, wg_ref, FFN_CONV_WIDTH, pad) + bg_ref[...]
            cv = _conv_block(xv_ref, x_t0, wv_ref, FFN_CONV_WIDTH, pad) + bv_ref[...]
            f_ref[pl.ds(t0, TIME_BLOCK), :] = (_gelu(cg) * cv).astype(BF16)

        block(0, hg_ref, hv_ref, 0, FFN_PAD)
        _time_loop(S, lambda t0: block(t0, ug_ref, uv_ref, t0, 0), skip_first=1)

    seq = lambda off: pl.BlockSpec((S, LANES), lambda i: (0, off + i))
    wsp = lambda off: pl.BlockSpec((FFN_CONV_WIDTH, LANES), lambda i: (0, off + i))
    bsp = lambda off: pl.BlockSpec((1, LANES), lambda i: (0, off + i))
    return pl.pallas_call(
        body, name="ffn_conv_geglu", grid=(nt,),
        in_specs=[seq(0), seq(nt), wsp(0), wsp(nt), bsp(0), bsp(nt)],
        out_specs=seq(0), out_shape=jax.ShapeDtypeStruct((S, C), BF16),
        scratch_shapes=[pltpu.VMEM((FFN_PAD + TIME_BLOCK, LANES), F32)] * 2,
        compiler_params=_params(("parallel",)),
    )(u, u, w, w, b, b)


def _ffn_bwd_call(u, w, b, df):
    S, C2 = u.shape
    C = C2 // 2
    nt = C // LANES

    def body(ug_ref, uv_ref, wg_ref, wv_ref, bg_ref, bv_ref, df_ref,
             du_ref, dwg_ref, dwv_ref, dbg_ref, dbv_ref,
             hg_ref, hv_ref, dg_ref, dv_ref, dwg_acc, dwv_acc, dbg_acc, dbv_acc):
        zeros = jnp.zeros((FFN_PAD, LANES), F32)
        _fill_head(hg_ref, ug_ref, FFN_PAD)
        _fill_head(hv_ref, uv_ref, FFN_PAD)
        dg_ref[S:, :] = zeros
        dv_ref[S:, :] = zeros
        dwg_acc[...] = jnp.zeros_like(dwg_acc)
        dwv_acc[...] = jnp.zeros_like(dwv_acc)
        dbg_acc[...] = jnp.zeros_like(dbg_acc)
        dbv_acc[...] = jnp.zeros_like(dbv_acc)

        def first(t0, xg_ref, xv_ref, x_t0, pad):
            rows = pl.ds(t0, TIME_BLOCK)
            cg = _conv_block(xg_ref, x_t0, wg_ref, FFN_CONV_WIDTH, pad) + bg_ref[...]
            cv = _conv_block(xv_ref, x_t0, wv_ref, FFN_CONV_WIDTH, pad) + bv_ref[...]
            dfb = df_ref[rows, :]
            gelu, gelu_grad = _gelu_and_grad(cg)
            dcg = dfb * cv * gelu_grad
            dcv = dfb * gelu
            dg_ref[rows, :] = dcg
            dv_ref[rows, :] = dcv
            _conv_weight_grad(xg_ref, x_t0, dcg, dwg_acc, FFN_CONV_WIDTH, pad)
            _conv_weight_grad(xv_ref, x_t0, dcv, dwv_acc, FFN_CONV_WIDTH, pad)
            dbg_acc[...] += jnp.sum(dcg, axis=0, keepdims=True)
            dbv_acc[...] += jnp.sum(dcv, axis=0, keepdims=True)

        def second(t0):
            rows = pl.ds(t0, TIME_BLOCK)
            du_ref[0, rows, :] = _conv_transpose_block(dg_ref, t0, wg_ref, FFN_CONV_WIDTH).astype(BF16)
            du_ref[1, rows, :] = _conv_transpose_block(dv_ref, t0, wv_ref, FFN_CONV_WIDTH).astype(BF16)

        first(0, hg_ref, hv_ref, 0, FFN_PAD)
        _time_loop(S, lambda t0: first(t0, ug_ref, uv_ref, t0, 0), skip_first=1)
        _time_loop(S, second)
        dwg_ref[...] = dwg_acc[...]
        dwv_ref[...] = dwv_acc[...]
        dbg_ref[...] = dbg_acc[...]
        dbv_ref[...] = dbv_acc[...]

    seq = lambda off: pl.BlockSpec((S, LANES), lambda i: (0, off + i))
    wsp = lambda off: pl.BlockSpec((FFN_CONV_WIDTH, LANES), lambda i: (0, off + i))
    bsp = lambda off: pl.BlockSpec((1, LANES), lambda i: (0, off + i))
    return pl.pallas_call(
        body, name="ffn_conv_geglu_bwd", grid=(nt,),
        in_specs=[seq(0), seq(nt), wsp(0), wsp(nt), bsp(0), bsp(nt), seq(0)],
        out_specs=[pl.BlockSpec((2, S, LANES), lambda i: (0, 0, i)),
                   pl.BlockSpec((SUBLANES, LANES), lambda i: (0, i)), pl.BlockSpec((SUBLANES, LANES), lambda i: (0, i)),
                   bsp(0), bsp(0)],
        out_shape=[jax.ShapeDtypeStruct((2, S, C), BF16)] + [jax.ShapeDtypeStruct((SUBLANES, C), F32)] * 2
        + [jax.ShapeDtypeStruct((1, C), F32)] * 2,
        scratch_shapes=[pltpu.VMEM((FFN_PAD + TIME_BLOCK, LANES), F32)] * 2 + [pltpu.VMEM((S + FFN_PAD, LANES), F32)] * 2
        + [pltpu.VMEM((SUBLANES, LANES), F32)] * 2
        + [pltpu.VMEM((1, LANES), F32)] * 2,
        compiler_params=_params(("parallel",)),
    )(u, u, w, w, b, b, df)


def _adamw(w_ref, g_ref, m_ref, v_ref, d_ref, mo_ref, vo_ref):
    gv = g_ref[...]
    mn = ADAM_B1 * m_ref[...] + (1.0 - ADAM_B1) * gv
    vn = ADAM_B2 * v_ref[...] + (1.0 - ADAM_B2) * (gv * gv)
    mo_ref[...] = mn
    vo_ref[...] = vn
    m_hat = mn * (1.0 / (1.0 - ADAM_B1 ** ADAM_STEP))
    v_hat = vn * (1.0 / (1.0 - ADAM_B2 ** ADAM_STEP))
    d_ref[...] = -ADAM_LR * (m_hat / (jnp.sqrt(v_hat) + ADAM_EPS) + ADAM_WD * w_ref[...])


def _adamw_call(w, g, m, v, name):
    R, C = w.shape
    tr = _row_tile(R, C)

    def body(w_ref, g_ref, m_ref, v_ref, go_ref, d_ref, mo_ref, vo_ref):
        go_ref[...] = g_ref[...]
        _adamw(w_ref, g_ref, m_ref, v_ref, d_ref, mo_ref, vo_ref)

    spec = pl.BlockSpec((tr, C), lambda i: (i, 0))
    return pl.pallas_call(
        body, name=name, grid=(R // tr,),
        in_specs=[spec] * 4, out_specs=[spec] * 4,
        out_shape=[jax.ShapeDtypeStruct((R, C), F32)] * 4,
        compiler_params=_params(("parallel",)),
    )(w, g, m, v)


def _adamw_small_call(ws, gs, ms, vs):
    n = len(ws)

    def body(*refs):
        w_refs, g_refs, m_refs, v_refs, d_refs, mo_refs, vo_refs = (refs[i * n:(i + 1) * n] for i in range(7))
        for i in range(n):
            _adamw(w_refs[i], g_refs[i], m_refs[i], v_refs[i], d_refs[i], mo_refs[i], vo_refs[i])

    whole = pl.BlockSpec(memory_space=pltpu.VMEM)
    outs = pl.pallas_call(
        body, name="adamw_small",
        in_specs=[whole] * (4 * n), out_specs=[whole] * (3 * n),
        out_shape=[jax.ShapeDtypeStruct(w.shape, F32) for w in ws] * 3,
    )(*ws, *gs, *ms, *vs)
    return outs[:n], outs[n:2 * n], outs[2 * n:]


def _position():
    return lax.axis_index("x"), lax.axis_index("y"), lax.axis_index("c")


def _chip_peers(x, y):
    return [(x, 1 - y), (1 - x, y), (1 - x, 1 - y)]


def _half_rows(ref, core, rows):
    h = rows // 2
    start = pl.multiple_of(core * h, 16)
    return ref.at[pl.ds(start, h), :] if len(ref.shape) == 2 else ref.at[:, pl.ds(start, h), :]


def _shard_half(ref, shard, core, rows):
    h = rows // 2
    return ref.at[shard, pl.ds(pl.multiple_of(core * h, 16), h), :]


ANY = pl.BlockSpec(memory_space=pl.ANY)


def _first_hop_copies(srcs, lands):
    x, y, c = _position()
    chip = 2 * x + y
    targets = [(px, py, c) for px, py in _chip_peers(x, y)] + [(x, y, 1 - c)]
    rows = srcs[0].shape[0]
    out = []
    for i, (s, l) in enumerate(zip(srcs, lands)):
        for k, dev in enumerate(targets):
            if i == 0 and k < 3:
                out.append((_half_rows(s, c, rows), _shard_half(l, chip, c, rows), dev, k))
            else:
                out.append((s, l.at[chip], dev, len(targets) * i + k))
    return out


def _second_hop_copies(srcs, lands):
    x, y, c = _position()
    rows = lands[0].shape[1]
    out = []
    for k, (px, py) in enumerate(_chip_peers(x, y)):
        half = _shard_half(lands[0], 2 * px + py, c, rows)
        out.append((half, half, (x, y, 1 - c), k))
    return out


HBM_SPEC = pl.BlockSpec(memory_space=pltpu.HBM)
SEM_SPEC = pl.BlockSpec(memory_space=pltpu.SEMAPHORE)
DATAFLOW = pltpu.SideEffectType.DATAFLOW_SIDE_EFFECTING


def _in_hbm(a):
    return pltpu.with_memory_space_constraint(a, pltpu.HBM)


def _split_start(name, groups, after, carry=None):
    spans, arrays = [], []
    for srcs, lands, _, _ in groups:
        spans.append((len(arrays), len(srcs), len(lands)))
        arrays += list(srcs) + list(lands)
    if carry is not None:
        arrays.append(carry)
    na, ng = len(arrays), len(groups)

    def body(*refs):
        sems, token = refs[na + 1:na + 1 + 2 * ng], refs[-1]
        for g, (_, _, _, copies) in enumerate(groups):
            off, ns, nl = spans[g]
            for src, dst, dev, idx in copies(refs[off:off + ns], refs[off + ns:off + ns + nl]):
                pltpu.make_async_remote_copy(src_ref=src, dst_ref=dst, send_sem=sems[2 * g].at[idx], recv_sem=sems[2 * g + 1].at[idx],
                                             device_id=dev, device_id_type=MESH).start()
        token[...] = jnp.zeros_like(token)

    outs = pl.pallas_call(
        body, name=name,
        in_specs=[HBM_SPEC] * na + [ANY],
        out_specs=[SEM_SPEC] * (2 * ng) + [HBM_SPEC] * na + [pl.BlockSpec(memory_space=pltpu.VMEM)],
        out_shape=[pltpu.SemaphoreType.DMA((n_sems,)) for _, _, n_sems, _ in groups for _ in range(2)]
        + [pltpu.HBM(a.shape, a.dtype) for a in arrays] + [jax.ShapeDtypeStruct((SUBLANES, LANES), F32)],
        input_output_aliases={i: 2 * ng + i for i in range(na)},
        compiler_params=pltpu.CompilerParams(has_side_effects=DATAFLOW),
    )(*[_in_hbm(a) for a in arrays], after)
    started = []
    for g, (off, ns, nl) in enumerate(spans):
        thru = outs[2 * ng + off:2 * ng + off + ns + nl]
        started.append(dict(send=outs[2 * g], recv=outs[2 * g + 1], srcs=list(thru[:ns]), lands=list(thru[ns:]),
                            tile=outs[-1], token=outs[-1][0, 0], carry=None if carry is None else outs[2 * ng + na - 1]))
    return started


def _split_wait(name, started, copies, after):
    n, m = len(started["srcs"]), len(started["lands"])

    def body(*refs):
        src_refs, land_refs = refs[:n], refs[n:n + m]
        send_sem, recv_sem = refs[n + m], refs[n + m + 1]
        for src, dst, dev, idx in copies(src_refs, land_refs):
            cp = pltpu.make_async_remote_copy(src_ref=src, dst_ref=dst, send_sem=send_sem.at[idx], recv_sem=recv_sem.at[idx],
                                              device_id=dev, device_id_type=MESH)
            cp.wait_send()
            cp.wait_recv()

    arrays = started["srcs"] + started["lands"]
    outs = pl.pallas_call(
        body, name=name,
        in_specs=[HBM_SPEC] * (n + m) + [SEM_SPEC, SEM_SPEC, ANY],
        out_specs=[HBM_SPEC] * (n + m),
        out_shape=[pltpu.HBM(a.shape, a.dtype) for a in arrays],
        input_output_aliases={i: i for i in range(n + m)},
        compiler_params=pltpu.CompilerParams(has_side_effects=DATAFLOW),
    )(*arrays, started["send"], started["recv"], after)
    return list(outs)


def _gather_copies(srcs, lands):
    x, y, c = _position()
    chip = 2 * x + y
    targets = [(px, py, c) for px, py in _chip_peers(x, y)] + [(x, y, 1 - c)]
    return [(s, l.at[chip], dev, len(targets) * i + k) for i, (s, l) in enumerate(zip(srcs, lands)) for k, dev in enumerate(targets)]


def _sibling_copies(srcs, lands):
    x, y, c = _position()
    return [(_half_rows(srcs[0], 1 - c, srcs[0].shape[1]), lands[0], (x, y, 1 - c), 0)]


def _exchange_copies(srcs, lands):
    x, y, c = _position()
    return [(srcs[0].at[2 * px + py], lands[0].at[k], (px, py, c), k) for k, (px, py) in enumerate(_chip_peers(x, y))]


def _pair_sum_call(grad, recv, core, name):
    _, h, B = recv.shape
    tr = _row_tile(h, B)

    def body(core_ref, g_ref, r_ref, o_ref, ob_ref):
        s = g_ref[...] + r_ref[...]
        o_ref[...] = s
        ob_ref[...] = s.astype(BF16)

    g_spec = pl.BlockSpec((None, tr, B), lambda q, i, core_ref: (q, core_ref[0] * (h // tr) + i, 0))
    spec = pl.BlockSpec((None, tr, B), lambda q, i, core_ref: (q, i, 0))
    return pl.pallas_call(
        body, name=name,
        grid_spec=pltpu.PrefetchScalarGridSpec(num_scalar_prefetch=1, grid=(N_CHIPS, h // tr), in_specs=[g_spec, spec],
                                               out_specs=[spec, spec]),
        out_shape=[jax.ShapeDtypeStruct(recv.shape, F32), jax.ShapeDtypeStruct(recv.shape, BF16)],
        compiler_params=_params(("parallel", "parallel")),
    )(core, grad, recv)


def _chip_sum_call(partial, recv, chip_core, name):
    _, h, B = recv.shape
    tr = _row_tile(h, B)

    def body(cc_ref, p_ref, r_ref, o_ref):
        o_ref[...] = ((p_ref[...] + r_ref[0].astype(F32)) + r_ref[1].astype(F32)) + r_ref[2].astype(F32)

    return pl.pallas_call(
        body, name=name,
        grid_spec=pltpu.PrefetchScalarGridSpec(
            num_scalar_prefetch=1, grid=(h // tr,),
            in_specs=[pl.BlockSpec((None, tr, B), lambda i, cc_ref: (cc_ref[0], i, 0)),
                      pl.BlockSpec((3, tr, B), lambda i, cc_ref: (0, i, 0))],
            out_specs=pl.BlockSpec((tr, B), lambda i, cc_ref: (cc_ref[1] * (h // tr) + i, 0))),
        out_shape=jax.ShapeDtypeStruct((2 * h, B), F32),
        compiler_params=_params(("parallel",)),
    )(chip_core, partial, recv)


def _sibling_assemble_call(shards, name="grad_sibling_assemble"):
    n = len(shards)

    def body(*refs):
        ins, outs = refs[:n], refs[n:2 * n]
        send_sems, recv_sems = refs[2 * n:]
        x, y, c = _position()
        copies = []
        for i in range(n):
            rows = shards[i].shape[0]
            cp = pltpu.make_async_remote_copy(src_ref=_half_rows(ins[i], c, rows), dst_ref=_half_rows(outs[i], c, rows),
                                              send_sem=send_sems.at[i], recv_sem=recv_sems.at[i],
                                              device_id=(x, y, 1 - c), device_id_type=MESH)
            cp.start()
            copies.append(cp)
        for cp in copies:
            cp.wait()

    return pl.pallas_call(
        body, name=name,
        in_specs=[ANY] * n, out_specs=[ANY] * n,
        out_shape=[jax.ShapeDtypeStruct(s.shape, F32) for s in shards],
        input_output_aliases={i: i for i in range(n)},
        scratch_shapes=[pltpu.SemaphoreType.DMA((n,)), pltpu.SemaphoreType.DMA((n,))],
    )(*shards)


N_DEVICES = 8


def _allsum_copies(srcs, lands):
    x, y, c = _position()
    me = 4 * x + 2 * y + c
    out = []
    for k in range(1, N_DEVICES):
        peer = (1 - x if k & 4 else x, 1 - y if k & 2 else y, 1 - c if k & 1 else c)
        out.append((srcs[0], lands[0].at[me], peer, k - 1))
    return out


def _ordered_sum_call(mine, landed, me):
    rows = mine.shape[0]

    def body(me_ref, x_ref, l_ref, o_ref):
        acc = jnp.where(me_ref[0] == 0, x_ref[...], l_ref[0])
        for d in range(1, N_DEVICES):
            acc = acc + jnp.where(me_ref[0] == d, x_ref[...], l_ref[d])
        o_ref[...] = acc

    return pl.pallas_call(
        body, name="small_grad_sum",
        in_specs=[pl.BlockSpec(memory_space=pltpu.SMEM), pl.BlockSpec(memory_space=pltpu.VMEM), pl.BlockSpec(memory_space=pltpu.VMEM)],
        out_specs=pl.BlockSpec(memory_space=pltpu.VMEM),
        out_shape=jax.ShapeDtypeStruct((rows, LANES), F32),
    )(me, mine, landed)


def _pack(arrays):
    flat = jnp.concatenate([a.reshape(-1).astype(F32) for a in arrays])
    rows = -(-flat.shape[0] // LANES)
    rows = -(-rows // SUBLANES) * SUBLANES
    flat = jnp.pad(flat, (0, rows * LANES - flat.shape[0]))
    return flat.reshape(rows, LANES)


def _unpack(packed, shapes):
    flat = packed.reshape(-1)
    out, off = [], 0
    for shp in shapes:
        size = int(np.prod(shp))
        out.append(flat[off:off + size].reshape(shp))
        off += size
    return out


def _local_step(xs, target, P, late_weights, on_grad):
    S, D = xs.shape
    qkv_width = 3 * N_HEADS * HEAD_DIM
    glu_col0, gate_col0 = qkv_width, qkv_width + 2 * D
    shard_major = lambda g: g.reshape(N_CHIPS, g.shape[0] // N_CHIPS, g.shape[1])

    h1 = _rms_fwd_call(xs, P["norm_mix_pre"])
    buckets = _bucket_tables()
    bias = _bias_table_call(P["rel_bias"] + 0.0 * h1[0, 0].astype(F32), buckets)
    P = dict(P, **late_weights("in", bias))
    proj = _matmul(h1, P["w_in"], "nn", "proj_in")
    parts = []
    for g in range(N_GROUPS):
        parts += _attn_fwd_call(proj, bias, g)
    a, a_bf, lse = _attn_merge_call(parts)
    P = dict(P, **late_weights("mix", a_bf))
    y_a = _matmul(a_bf, P["w_attn_out"], "nn", "attn_out")
    c1 = _conv_fwd_call(proj, glu_col0, P["conv_dw_w"], P["conv_dw_b"])
    cact = _ln_silu_call(c1, P["conv_ln_g"], P["conv_ln_b"])
    y_c = _matmul(cact, P["conv_pw_w"], "nn", "conv_pw")
    mixed = _mix_call(proj, gate_col0, P["b_gate"], y_a, y_c)
    out = _matmul(mixed, P["w_out"], "nn", "mix_out")
    x1, h2 = _res1_call(xs, out, P["norm_mix_post"], P["norm_ffn_pre"])
    P = dict(P, **late_weights("up", h2))
    u = _matmul(h2, P["w_up"], "nn", "ffn_up")
    f = _ffn_fwd_call(u, P["ffn_conv_w"], P["ffn_conv_b"])
    P = dict(P, **late_weights("down", f))
    yff = _matmul(f, P["w_down"], "nn", "ffn_down")
    loss_tile, dx2, dyff, dg_ffn_post = _loss_call(yff, x1, P["norm_ffn_post"], target)

    G = {}
    G["norm_ffn_post"] = dg_ffn_post
    zero = on_grad("w_down", shard_major(_matmul(f, dyff, "tn", "ffn_down_dw")))
    df = _matmul(dyff, P["w_down"], "nt", "ffn_down_dx")
    du, dwg, dwv, dbg, dbv = _ffn_bwd_call(u, P["ffn_conv_w"], P["ffn_conv_b"] + zero, df)
    G["ffn_conv_w"] = jnp.concatenate([dwg[:FFN_CONV_WIDTH], dwv[:FFN_CONV_WIDTH]], axis=1)
    G["ffn_conv_b"] = jnp.concatenate([dbg, dbv], axis=1)
    zero = on_grad("w_up", _matmul(h2, du, "tn", "ffn_up_dw", out_shards=True))
    dh2 = _matmul(du, P["w_up"], "nt", "ffn_up_dx")
    dx1, dout, G["norm_ffn_pre"], G["norm_mix_post"] = _mid_bwd_call(x1, P["norm_ffn_pre"] + zero, dh2, dx2, out, P["norm_mix_post"])
    zero = on_grad("w_out", shard_major(_matmul(mixed, dout, "tn", "mix_out_dw")))
    dmixed = _matmul(dout, P["w_out"], "nt", "mix_out_dx")
    dya, dyc, dga, dgc, dba, dbc = _mix_bwd_call(dmixed, proj, gate_col0, P["b_gate"] + zero, y_a, y_c)
    G["b_gate"] = jnp.concatenate([dba, dbc], axis=1)
    zero = on_grad("w_attn_out", _matmul(a_bf, dya, "tn", "attn_out_dw", out_shards=True))
    zero = zero + on_grad("conv_pw_w", shard_major(_matmul(cact, dyc, "tn", "conv_pw_dw")))
    da = _matmul(dya, P["w_attn_out"], "nt", "attn_out_dx")
    dcact = _matmul(dyc, P["conv_pw_w"], "nt", "conv_pw_dx")
    dc1, G["conv_ln_g"], G["conv_ln_b"] = _ln_silu_bwd_call(c1, P["conv_ln_g"] + zero, P["conv_ln_b"], dcact)
    dval, dgate, dw_dw, G["conv_dw_b"] = _conv_bwd_call(proj, glu_col0, P["conv_dw_w"], dc1)
    G["conv_dw_w"] = dw_dw[:CONV_WIDTH]
    delta = _attn_delta_call(a, da)
    dqs, dks, dvs, dbs = [], [], [], []
    for g in range(N_GROUPS):
        dq, dk, dv, db = _attn_bwd_call(proj, bias, da, lse, delta, g)
        dqs.append(dq)
        dks.append(dk)
        dvs.append(dv)
        dbs.append(db)
    G["rel_bias"] = _bias_grad_call(jnp.concatenate(dbs, axis=0), buckets)
    dproj = _dproj_call(dqs + dks + dvs, [dval, dgate, dga, dgc])
    zero, dproj = on_grad("w_in", _matmul(h1, dproj, "tn", "proj_in_dw", out_shards=True), carry=dproj)
    dh1 = _matmul(dproj, P["w_in"], "nt", "proj_in_dx")
    zero = zero + on_grad(None, dh1)
    grad_x, G["norm_mix_pre"] = _in_bwd_call(xs, P["norm_mix_pre"] + zero, dh1, dx1)
    return loss_tile, grad_x, G


def kernel(x, w_in, b_gate, rel_bias, w_attn_out, conv_dw_w, conv_dw_b, conv_ln_g, conv_ln_b, conv_pw_w, w_out, norm_mix_pre, norm_mix_post, norm_ffn_pre, norm_ffn_post, w_up, ffn_conv_w, ffn_conv_b, w_down, loss_target, m_w_in, m_b_gate, m_rel_bias, m_w_attn_out, m_conv_dw_w, m_conv_dw_b, m_conv_ln_g, m_conv_ln_b, m_conv_pw_w, m_w_out, m_norm_mix_pre, m_norm_mix_post, m_norm_ffn_pre, m_norm_ffn_post, m_w_up, m_ffn_conv_w, m_ffn_conv_b, m_w_down, v_w_in, v_b_gate, v_rel_bias, v_w_attn_out, v_conv_dw_w, v_conv_dw_b, v_conv_ln_g, v_conv_ln_b, v_conv_pw_w, v_w_out, v_norm_mix_pre, v_norm_mix_post, v_norm_ffn_pre, v_norm_ffn_post, v_w_up, v_ffn_conv_w, v_ffn_conv_b, v_w_down):
    weights = dict(w_in=w_in, b_gate=b_gate, rel_bias=rel_bias, w_attn_out=w_attn_out, conv_dw_w=conv_dw_w, conv_dw_b=conv_dw_b,
                   conv_ln_g=conv_ln_g, conv_ln_b=conv_ln_b, conv_pw_w=conv_pw_w, w_out=w_out, norm_mix_pre=norm_mix_pre,
                   norm_mix_post=norm_mix_post, norm_ffn_pre=norm_ffn_pre, norm_ffn_post=norm_ffn_post, w_up=w_up,
                   ffn_conv_w=ffn_conv_w, ffn_conv_b=ffn_conv_b, w_down=w_down)
    m_in = dict(w_in=m_w_in, b_gate=m_b_gate, rel_bias=m_rel_bias, w_attn_out=m_w_attn_out, conv_dw_w=m_conv_dw_w,
                conv_dw_b=m_conv_dw_b, conv_ln_g=m_conv_ln_g, conv_ln_b=m_conv_ln_b, conv_pw_w=m_conv_pw_w, w_out=m_w_out,
                norm_mix_pre=m_norm_mix_pre, norm_mix_post=m_norm_mix_post, norm_ffn_pre=m_norm_ffn_pre,
                norm_ffn_post=m_norm_ffn_post, w_up=m_w_up, ffn_conv_w=m_ffn_conv_w, ffn_conv_b=m_ffn_conv_b, w_down=m_w_down)
    v_in = dict(w_in=v_w_in, b_gate=v_b_gate, rel_bias=v_rel_bias, w_attn_out=v_w_attn_out, conv_dw_w=v_conv_dw_w,
                conv_dw_b=v_conv_dw_b, conv_ln_g=v_conv_ln_g, conv_ln_b=v_conv_ln_b, conv_pw_w=v_conv_pw_w, w_out=v_w_out,
                norm_mix_pre=v_norm_mix_pre, norm_mix_post=v_norm_mix_post, norm_ffn_pre=v_norm_ffn_pre,
                norm_ffn_post=v_norm_ffn_post, w_up=v_w_up, ffn_conv_w=v_ffn_conv_w, ffn_conv_b=v_ffn_conv_b, w_down=v_w_down)
    names = list(weights)
    xi, yi, ci = _position()
    chip = 2 * xi + yi
    core_arr = jnp.reshape(ci, (1,)).astype(jnp.int32)

    xs = x[0]
    target = loss_target[0]
    S, D = xs.shape

    big = ["w_in", "w_attn_out", "conv_pw_w", "w_out", "w_up", "w_down"]
    row_sharded = ("conv_pw_w", "w_out", "w_down")
    bf16_shard = {k: weights[k][0].astype(BF16) for k in big}
    natural = lambda k, g: g.reshape(-1, g.shape[2]) if k in row_sharded else g
    first_srcs = [bf16_shard["w_in"], conv_dw_w[0], ffn_conv_w[0]]
    first_lands = [lax.empty((N_CHIPS,) + s.shape, s.dtype) for s in first_srcs]
    (first_hop,) = _split_start("gather_in_start", [(first_srcs, first_lands, 4 * len(first_srcs), _first_hop_copies)], core_arr)
    launched = first_hop["token"]
    late_sets = dict(mix=["w_attn_out", "conv_pw_w", "w_out"], up=["w_up"], down=["w_down"])
    late_groups = []
    for keys in late_sets.values():
        srcs = [bf16_shard[k] for k in keys]
        late_groups.append((srcs, [lax.empty((N_CHIPS,) + s.shape, BF16) for s in srcs], 4 * len(keys), _gather_copies))
    started = {}

    def late_weights(tag, after):
        if tag == "in":
            w_in_halves, dw4, fc4 = _split_wait("gather_in_wait", first_hop, _first_hop_copies, after)[len(first_srcs):]
            second_hop, *late = _split_start("gather_in_pass_start", [([], [w_in_halves], 3, _second_hop_copies)] + late_groups, dw4)
            started.update(zip(late_sets, late))
            (w_in_full,) = _split_wait("gather_in_pass_wait", second_hop, _second_hop_copies, second_hop["tile"])
            return dict(w_in=w_in_full, conv_dw_w=jnp.concatenate(list(dw4), axis=1), ffn_conv_w=jnp.concatenate(list(fc4), axis=1))
        landed = _split_wait(f"gather_{tag}_wait", started[tag], _gather_copies, after)[len(late_sets[tag]):]
        return {k: natural(k, g) for k, g in zip(late_sets[tag], landed)}

    chip_core = jnp.stack([chip, ci]).astype(jnp.int32)
    exchanging, pending = {}, {}

    def launch(tag, g3, after, carry=None):
        keys, groups, partial = [], [], {}
        for k in list(exchanging):
            gk, r1 = _split_wait(f"sibling_exchange_wait_{k}", exchanging.pop(k), _sibling_copies, after)
            partial[k], s16 = _pair_sum_call(gk, r1, core_arr, f"pair_sum_{k}")
            keys.append(k)
            groups.append(([s16], [lax.empty((3,) + s16.shape[1:], BF16)], 3, _exchange_copies))
        if g3 is not None:
            groups.append(([g3], [lax.empty((N_CHIPS, g3.shape[1] // 2, g3.shape[2]), F32)], 1, _sibling_copies))
        begun = _split_start(f"grad_exchange_start_{tag}", groups, core_arr, carry)
        for k, st in zip(keys, begun):
            pending[k] = (partial[k], st)
        if g3 is not None:
            exchanging[tag] = begun[-1]
        return begun[0]["token"] if carry is None else (begun[0]["token"], begun[0]["carry"])

    def on_grad(k, g3, carry=None):
        if k is None:
            return launch("last", None, g3[:SUBLANES, :LANES])
        return launch(k, g3, g3[0, :SUBLANES, :LANES], carry)

    def finish(keys, after, tag):
        halves = []
        for k in keys:
            s32, st = pending[k]
            recv2 = _split_wait(f"chip_exchange_wait_{k}", st, _exchange_copies, after)[1]
            halves.append(_chip_sum_call(s32, recv2, chip_core, f"chip_sum_{k}"))
        return dict(zip(keys, _sibling_assemble_call(halves, f"grad_sibling_assemble_{tag}")))

    P = dict(b_gate=b_gate, rel_bias=rel_bias, conv_dw_b=conv_dw_b, conv_ln_g=conv_ln_g, conv_ln_b=conv_ln_b,
             norm_mix_pre=norm_mix_pre + launched, norm_mix_post=norm_mix_post, norm_ffn_pre=norm_ffn_pre,
             norm_ffn_post=norm_ffn_post, ffn_conv_b=ffn_conv_b)
    loss_tile, grad_x, G = _local_step(xs, target, P, late_weights, on_grad)

    small = [k for k in names if k not in big]
    packed = _pack([loss_tile[:1]] + [G[k] for k in small])
    (allsum,) = _split_start("small_grad_allsum_start",
                             [([packed], [jnp.zeros((N_DEVICES,) + packed.shape, F32)], N_DEVICES - 1, _allsum_copies)], core_arr)

    reduced, grads, deltas, new_m, new_v = {}, {}, {}, {}, {}

    def update(keys):
        for k in keys:
            gk, d, mn, vn = _adamw_call(weights[k][0], reduced[k], m_in[k][0], v_in[k][0], f"adamw_{k}")
            grads[k], deltas[k], new_m[k], new_v[k] = gk[None], d[None], mn[None], vn[None]

    others = [k for k in big if k != "w_in"]
    reduced.update(finish(others, allsum["tile"], "others"))
    update(others)
    reduced.update(finish(["w_in"], deltas["w_up"], "w_in"))
    update(["w_in"])

    me = jnp.reshape(4 * xi + 2 * yi + ci, (1,)).astype(jnp.int32)
    mine, landed = _split_wait("small_grad_allsum_wait", allsum, _allsum_copies, deltas["w_in"])
    summed_block = _ordered_sum_call(mine, landed, me)
    loss_row, *summed = _unpack(summed_block, [(1, LANES)] + [G[k].shape for k in small])
    loss = loss_row[0, 0]
    for k, gsum in zip(small, summed):
        if k in ("conv_dw_w", "ffn_conv_w"):
            cols = weights[k].shape[2]
            reduced[k] = lax.dynamic_slice_in_dim(gsum, chip * cols, cols, axis=1)
        else:
            reduced[k] = gsum
    for k in small:
        grads[k] = reduced[k].reshape(weights[k].shape)
    ds, mns, vns = _adamw_small_call([weights[k] for k in small], [grads[k] for k in small],
                                     [m_in[k] for k in small], [v_in[k] for k in small])
    deltas.update(zip(small, ds))
    new_m.update(zip(small, mns))
    new_v.update(zip(small, vns))

    return (loss, grad_x[None], *[grads[k] for k in names], *[deltas[k] for k in names],
            *[new_m[k] for k in names], *[new_v[k] for k in names])
```

```python
import functools
import math

import jax
import jax.numpy as jnp
import numpy as np
from jax import lax
from jax.experimental import pallas as pl
from jax.experimental.pallas import tpu as pltpu

F32 = jnp.float32
BF16 = jnp.bfloat16
MESH = pl.DeviceIdType.MESH

HEAD_DIM = 128
HEADS_PER_GROUP = 4
DILATED_PATTERNS = ((128, 1), (512, 4), (2048, 16))
N_GROUPS = 3
N_HEADS = N_GROUPS * HEADS_PER_GROUP
SPAN = 128
GROUP_WIDTH = HEADS_PER_GROUP * HEAD_DIM
CONV_WIDTH = 31
FFN_CONV_WIDTH = 3
N_BUCKETS = 32
MAX_DISTANCE = 2048
RMS_EPS = 1e-6
LN_EPS = 1e-5
NEG_INF = -1e30
ADAM_LR = 0.001
ADAM_B1 = 0.9
ADAM_B2 = 0.999
ADAM_EPS = 1e-08
ADAM_WD = 0.01
ADAM_STEP = 10

LANES = 128
SUBLANES = 8
ROW_TILE = 512
GATE_ROWS, GATE_COLS = 512, 512
TIME_BLOCK = 128
CONV_PAD = 32
FFN_PAD = 8
VMEM_LIMIT = 56 << 20


def _params(sem=None, vmem=None):
    kw = {}
    if sem is not None:
        kw["dimension_semantics"] = sem
    if vmem is not None:
        kw["vmem_limit_bytes"] = vmem
    return pltpu.CompilerParams(**kw)


def _pick(n, cands):
    for c in cands:
        if n % c == 0:
            return c
    return n


ELEMENTWISE_TILE_BYTES = 3 << 19


def _row_tile(rows, cols):
    for align in (16, SUBLANES):
        fits = [t for t in range(align, rows + 1, align) if rows % t == 0 and t * cols * 4 <= ELEMENTWISE_TILE_BYTES]
        if fits:
            return max(fits)
    return SUBLANES


N_CHIPS = 4
M_TILES = (1024, 1408, 512, 256, 128)
N_TILES = (1024, 512, 1408, 256, 128)
K_TILES = (2176, 2048, 1408, 1024, 512, 256, 128)


def _matmul(a, b, mode, name, out_shards=False, tm=None):
    assert a.dtype == BF16 and b.dtype == BF16, (name, a.dtype, b.dtype)
    b3 = b.ndim == 3
    tn = tk = None
    halves = None
    if mode == "nn":
        M, K = a.shape
        N = b.shape[-1] * (N_CHIPS if b3 else 1)
        tn = b.shape[-1] if b3 else None
    elif mode == "nt":
        if a.ndim == 3:
            halves = a.shape[2]
        M, K = a.shape[-2], a.shape[-1] * (a.shape[0] if a.ndim == 3 else 1)
        N = b.shape[-2]
        tk = b.shape[-1] if b3 else None
    else:
        if b3:
            halves = b.shape[2]
        K, M = a.shape
        N = b.shape[-1] * (b.shape[0] if b3 else 1)
        tn = N // N_CHIPS if out_shards else None
    tm = tm or _pick(M, M_TILES)
    tn = tn or _pick(N, N_TILES)
    tk = tk or _pick(K, K_TILES)
    nk = K // tk
    dn = {"nn": (((1,), (0,)), ((), ())), "nt": (((1,), (1,)), ((), ())), "tn": (((0,), (0,)), ((), ()))}[mode]

    def body(a_ref, b_ref, o_ref):
        if nk == 1:
            o_ref[...] = lax.dot_general(a_ref[...], b_ref[...], dn, preferred_element_type=F32)
        else:
            @pl.when(pl.program_id(2) == 0)
            def _():
                o_ref[...] = jnp.zeros_like(o_ref)

            o_ref[...] += lax.dot_general(a_ref[...], b_ref[...], dn, preferred_element_type=F32)

    if mode == "tn":
        a_spec = pl.BlockSpec((tk, tm), lambda i, j, k: (k, i))
    elif halves:
        per = halves // tk
        a_spec = pl.BlockSpec((None, tm, tk), lambda i, j, k: (k // per, i, k % per))
    else:
        a_spec = pl.BlockSpec((tm, tk), lambda i, j, k: (i, k))
    if mode == "nn":
        b_spec = pl.BlockSpec((None, tk, tn), lambda i, j, k: (j, k, 0)) if b3 else pl.BlockSpec((tk, tn), lambda i, j, k: (k, j))
    elif mode == "nt":
        b_spec = pl.BlockSpec((None, tn, tk), lambda i, j, k: (k, j, 0)) if b3 else pl.BlockSpec((tn, tk), lambda i, j, k: (j, k))
    elif halves:
        per = halves // tn
        b_spec = pl.BlockSpec((None, tk, tn), lambda i, j, k: (j // per, k, j % per))
    else:
        b_spec = pl.BlockSpec((tk, tn), lambda i, j, k: (k, j))
    if out_shards:
        out_spec = pl.BlockSpec((None, tm, tn), lambda i, j, k: (j, i, 0))
        out_shape = jax.ShapeDtypeStruct((N_CHIPS, M, tn), F32)
    else:
        out_spec = pl.BlockSpec((tm, tn), lambda i, j, k: (i, j))
        out_shape = jax.ShapeDtypeStruct((M, N), F32)
    return pl.pallas_call(
        body, name=name, grid=(M // tm, N // tn, nk),
        in_specs=[a_spec, b_spec], out_specs=out_spec, out_shape=out_shape,
        compiler_params=_params(("parallel", "parallel", "arbitrary"), VMEM_LIMIT),
    )(a, b)


def _rms(x, g):
    r = lax.rsqrt(jnp.mean(x * x, axis=-1, keepdims=True) + RMS_EPS)
    return x * r * g


def _rms_bwd(x, g, dy):
    r = lax.rsqrt(jnp.mean(x * x, axis=-1, keepdims=True) + RMS_EPS)
    n = x * r
    dn = dy * g
    dx = r * (dn - n * jnp.mean(dn * n, axis=-1, keepdims=True))
    return dx, jnp.sum(dy * n, axis=0, keepdims=True)


def _sigmoid(x):
    return 1.0 / (1.0 + jnp.exp(-x))


_GELU_C = math.sqrt(2.0 / math.pi)


def _gelu(x):
    return 0.5 * x * (1.0 + jnp.tanh(_GELU_C * (x + 0.044715 * x * x * x)))


def _gelu_and_grad(x):
    x2 = x * x
    t = jnp.tanh(_GELU_C * x * (1.0 + 0.044715 * x2))
    half = 0.5 * (1.0 + t)
    return x * half, half + (0.5 * _GELU_C) * x * (1.0 - t * t) * (1.0 + (3.0 * 0.044715) * x2)


def _row_spec(width, col_block=0):
    return pl.BlockSpec((ROW_TILE, width), lambda i: (i, col_block))


def _vec_spec(width, col_block=0):
    return pl.BlockSpec((1, width), lambda i: (0, col_block))


def _accumulate(ref, part):
    @pl.when(pl.program_id(0) == 0)
    def _():
        ref[...] = part

    @pl.when(pl.program_id(0) > 0)
    def _():
        ref[...] += part


def _rms_fwd_call(x, g):
    S, D = x.shape

    def body(x_ref, g_ref, h_ref):
        h_ref[...] = _rms(x_ref[...], g_ref[...]).astype(BF16)

    return pl.pallas_call(
        body, name="rms_mix_pre", grid=(S // ROW_TILE,),
        in_specs=[_row_spec(D), _vec_spec(D)], out_specs=_row_spec(D),
        out_shape=jax.ShapeDtypeStruct((S, D), BF16),
        compiler_params=_params(("parallel",)),
    )(x, g)


def _ln_silu_call(c1, g, b):
    S, C = c1.shape

    def body(c_ref, g_ref, b_ref, o_ref):
        xv = c_ref[...]
        mu = jnp.mean(xv, axis=-1, keepdims=True)
        xc = xv - mu
        var = jnp.mean(xc * xc, axis=-1, keepdims=True)
        z = xc * lax.rsqrt(var + LN_EPS) * g_ref[...] + b_ref[...]
        o_ref[...] = (z * _sigmoid(z)).astype(BF16)

    return pl.pallas_call(
        body, name="conv_ln_silu", grid=(S // ROW_TILE,),
        in_specs=[_row_spec(C), _vec_spec(C), _vec_spec(C)], out_specs=_row_spec(C),
        out_shape=jax.ShapeDtypeStruct((S, C), BF16),
        compiler_params=_params(("parallel",)),
    )(c1, g, b)


def _ln_silu_bwd_call(c1, g, b, dc):
    S, C = c1.shape

    def body(c_ref, g_ref, b_ref, dc_ref, dx_ref, dg_ref, db_ref):
        xv = c_ref[...]
        mu = jnp.mean(xv, axis=-1, keepdims=True)
        xc = xv - mu
        rs = lax.rsqrt(jnp.mean(xc * xc, axis=-1, keepdims=True) + LN_EPS)
        xh = xc * rs
        z = xh * g_ref[...] + b_ref[...]
        sg = _sigmoid(z)
        dz = dc_ref[...] * (sg * (1.0 + z * (1.0 - sg)))
        dxh = dz * g_ref[...]
        dx_ref[...] = rs * (dxh - jnp.mean(dxh, axis=-1, keepdims=True) - xh * jnp.mean(dxh * xh, axis=-1, keepdims=True))
        _accumulate(dg_ref, jnp.sum(dz * xh, axis=0, keepdims=True))
        _accumulate(db_ref, jnp.sum(dz, axis=0, keepdims=True))

    return pl.pallas_call(
        body, name="conv_ln_silu_bwd", grid=(S // ROW_TILE,),
        in_specs=[_row_spec(C), _vec_spec(C), _vec_spec(C), _row_spec(C)],
        out_specs=[_row_spec(C), _vec_spec(C), _vec_spec(C)],
        out_shape=[jax.ShapeDtypeStruct((S, C), F32), jax.ShapeDtypeStruct((1, C), F32), jax.ShapeDtypeStruct((1, C), F32)],
        compiler_params=_params(("arbitrary",)),
    )(c1, g, b, dc)


def _mix_call(proj, gate_col0, b_gate, y_a, y_c):
    S, D = y_a.shape
    w = GATE_COLS
    nc = D // w
    ga0, gc0 = gate_col0 // w, (gate_col0 + D) // w

    def body(ga_ref, gc_ref, ba_ref, bc_ref, ya_ref, yc_ref, o_ref):
        o_ref[...] = (_sigmoid(ga_ref[...] + ba_ref[...]) * ya_ref[...]
                      + _sigmoid(gc_ref[...] + bc_ref[...]) * yc_ref[...]).astype(BF16)

    tile = lambda off: pl.BlockSpec((GATE_ROWS, w), lambda i, j: (i, off + j))
    vec = lambda off: pl.BlockSpec((1, w), lambda i, j: (0, off + j))
    return pl.pallas_call(
        body, name="gate_mix", grid=(S // GATE_ROWS, nc),
        in_specs=[tile(ga0), tile(gc0), vec(0), vec(nc), tile(0), tile(0)],
        out_specs=tile(0), out_shape=jax.ShapeDtypeStruct((S, D), BF16),
        compiler_params=_params(("parallel", "parallel")),
    )(proj, proj, b_gate, b_gate, y_a, y_c)


def _mix_bwd_call(dmixed, proj, gate_col0, b_gate, y_a, y_c):
    S, D = y_a.shape
    w = GATE_COLS
    nc = D // w
    ga0, gc0 = gate_col0 // w, (gate_col0 + D) // w

    def body(dm_ref, ga_ref, gc_ref, ba_ref, bc_ref, ya_ref, yc_ref, dya_ref, dyc_ref, dga_ref, dgc_ref, dba_ref, dbc_ref):
        dm = dm_ref[...]
        sa = _sigmoid(ga_ref[...] + ba_ref[...])
        sc = _sigmoid(gc_ref[...] + bc_ref[...])
        dya_ref[...] = (dm * sa).astype(BF16)
        dyc_ref[...] = (dm * sc).astype(BF16)
        dga = dm * ya_ref[...] * sa * (1.0 - sa)
        dgc = dm * yc_ref[...] * sc * (1.0 - sc)
        dga_ref[...] = dga.astype(BF16)
        dgc_ref[...] = dgc.astype(BF16)
        pa = jnp.sum(dga, axis=0, keepdims=True)
        pc = jnp.sum(dgc, axis=0, keepdims=True)

        @pl.when(pl.program_id(1) == 0)
        def _():
            dba_ref[...] = pa
            dbc_ref[...] = pc

        @pl.when(pl.program_id(1) > 0)
        def _():
            dba_ref[...] += pa
            dbc_ref[...] += pc

    tile = lambda off: pl.BlockSpec((GATE_ROWS, w), lambda j, i: (i, off + j))
    vec = lambda off: pl.BlockSpec((1, w), lambda j, i: (0, off + j))
    return pl.pallas_call(
        body, name="gate_mix_bwd", grid=(nc, S // GATE_ROWS),
        in_specs=[tile(0), tile(ga0), tile(gc0), vec(0), vec(nc), tile(0), tile(0)],
        out_specs=[tile(0), tile(0), tile(0), tile(0), vec(0), vec(0)],
        out_shape=[jax.ShapeDtypeStruct((S, D), BF16)] * 4 + [
                   jax.ShapeDtypeStruct((1, D), F32), jax.ShapeDtypeStruct((1, D), F32)],
        compiler_params=_params(("parallel", "arbitrary")),
    )(dmixed, proj, proj, b_gate, b_gate, y_a, y_c)


def _res1_call(x, out, g_post, g_pre):
    S, D = x.shape

    def body(x_ref, o_ref, gp_ref, gq_ref, x1_ref, h2_ref):
        x1 = x_ref[...] + _rms(o_ref[...], gp_ref[...])
        x1_ref[...] = x1
        h2_ref[...] = _rms(x1, gq_ref[...]).astype(BF16)

    return pl.pallas_call(
        body, name="residual_mix", grid=(S // ROW_TILE,),
        in_specs=[_row_spec(D), _row_spec(D), _vec_spec(D), _vec_spec(D)],
        out_specs=[_row_spec(D), _row_spec(D)],
        out_shape=[jax.ShapeDtypeStruct((S, D), F32), jax.ShapeDtypeStruct((S, D), BF16)],
        compiler_params=_params(("parallel",)),
    )(x, out, g_post, g_pre)


def _loss_call(y, x1, g_post, target):
    S, D = y.shape

    def body(y_ref, x1_ref, g_ref, t_ref, loss_ref, dx_ref, dy_ref, dg_ref):
        yv, gv = y_ref[...], g_ref[...]
        err = x1_ref[...] + _rms(yv, gv) - t_ref[...]
        dx2 = err * (1.0 / D)
        dx_ref[...] = dx2
        dy, dg = _rms_bwd(yv, gv, dx2)
        dy_ref[...] = dy.astype(BF16)
        _accumulate(dg_ref, dg)
        part = 0.5 * jnp.sum(jnp.mean(err * err, axis=-1, keepdims=True), axis=0, keepdims=True)
        _accumulate(loss_ref, jnp.broadcast_to(part, (SUBLANES, LANES)))

    return pl.pallas_call(
        body, name="residual_ffn_loss", grid=(S // ROW_TILE,),
        in_specs=[_row_spec(D), _row_spec(D), _vec_spec(D), _row_spec(D)],
        out_specs=[pl.BlockSpec((SUBLANES, LANES), lambda i: (0, 0)), _row_spec(D), _row_spec(D), _vec_spec(D)],
        out_shape=[jax.ShapeDtypeStruct((SUBLANES, LANES), F32), jax.ShapeDtypeStruct((S, D), F32),
                   jax.ShapeDtypeStruct((S, D), BF16), jax.ShapeDtypeStruct((1, D), F32)],
        compiler_params=_params(("arbitrary",)),
    )(y, x1, g_post, target)


def _mid_bwd_call(x1, g_pre, dh2, dx2, out, g_post):
    S, D = x1.shape

    def body(x1_ref, gq_ref, dh_ref, dx2_ref, o_ref, gp_ref, dx1_ref, do_ref, dgq_ref, dgp_ref):
        d, dgq = _rms_bwd(x1_ref[...], gq_ref[...], dh_ref[...])
        dx1 = dx2_ref[...] + d
        dx1_ref[...] = dx1
        do, dgp = _rms_bwd(o_ref[...], gp_ref[...], dx1)
        do_ref[...] = do.astype(BF16)
        _accumulate(dgq_ref, dgq)
        _accumulate(dgp_ref, dgp)

    return pl.pallas_call(
        body, name="residual_mix_bwd", grid=(S // ROW_TILE,),
        in_specs=[_row_spec(D), _vec_spec(D), _row_spec(D), _row_spec(D), _row_spec(D), _vec_spec(D)],
        out_specs=[_row_spec(D), _row_spec(D), _vec_spec(D), _vec_spec(D)],
        out_shape=[jax.ShapeDtypeStruct((S, D), F32), jax.ShapeDtypeStruct((S, D), BF16)] + [jax.ShapeDtypeStruct((1, D), F32)] * 2,
        compiler_params=_params(("arbitrary",)),
    )(x1, g_pre, dh2, dx2, out, g_post)


def _in_bwd_call(x, g, dh1, dx1):
    S, D = x.shape

    def body(x_ref, g_ref, dh_ref, dx1_ref, gx_ref, dg_ref):
        d, dg = _rms_bwd(x_ref[...], g_ref[...], dh_ref[...])
        gx_ref[...] = dx1_ref[...] + d
        _accumulate(dg_ref, dg)

    return pl.pallas_call(
        body, name="rms_mix_pre_bwd", grid=(S // ROW_TILE,),
        in_specs=[_row_spec(D), _vec_spec(D), _row_spec(D), _row_spec(D)],
        out_specs=[_row_spec(D), _vec_spec(D)],
        out_shape=[jax.ShapeDtypeStruct((S, D), F32), jax.ShapeDtypeStruct((1, D), F32)],
        compiler_params=_params(("arbitrary",)),
    )(x, g, dh1, dx1)


def _bucket_table(dilation):
    qi = np.arange(SPAN)[:, None]
    ki = np.arange(2 * SPAN)[None, :]
    dist = np.maximum(qi + SPAN - ki, 0) * dilation
    max_exact = N_BUCKETS // 2
    d = np.maximum(dist, 1).astype(np.float64)
    large = max_exact + (np.log(d / max_exact) / math.log(MAX_DISTANCE / max_exact) * (N_BUCKETS - max_exact)).astype(np.int32)
    large = np.minimum(large, N_BUCKETS - 1)
    return np.where(dist < max_exact, dist, large).astype(np.int32)


def _bucket_tables():
    return jnp.asarray(np.stack([_bucket_table(r) for _, r in DILATED_PATTERNS]))


def _bias_table_call(rel_bias, buckets):
    def body(rb_ref, bk_ref, o_ref):
        for h in range(N_HEADS):
            bk = bk_ref[h // HEADS_PER_GROUP]

            def step(b, acc):
                return jnp.where(bk == b, rb_ref[b, h], acc)

            o_ref[h] = lax.fori_loop(0, N_BUCKETS, step, jnp.zeros((SPAN, 2 * SPAN), F32))

    return pl.pallas_call(
        body, name="rel_bias_table",
        in_specs=[pl.BlockSpec(memory_space=pltpu.SMEM), pl.BlockSpec(memory_space=pltpu.VMEM)],
        out_specs=pl.BlockSpec(memory_space=pltpu.VMEM),
        out_shape=jax.ShapeDtypeStruct((N_HEADS, SPAN, 2 * SPAN), F32),
    )(rel_bias, buckets)


def _bias_grad_call(dbias, buckets):
    def body(db_ref, bk_ref, o_ref, rows_ref):
        for h in range(N_HEADS):
            bk = bk_ref[h // HEADS_PER_GROUP]
            dv = db_ref[h]

            def step(b, carry):
                rows_ref[h, b] = jnp.sum(jnp.where(bk == b, dv, 0.0), axis=0, keepdims=True)
                return carry

            lax.fori_loop(0, N_BUCKETS, step, 0)
        o_ref[...] = jnp.sum(rows_ref[...], axis=-1, keepdims=True)

    out = pl.pallas_call(
        body, name="rel_bias_grad",
        in_specs=[pl.BlockSpec(memory_space=pltpu.VMEM), pl.BlockSpec(memory_space=pltpu.VMEM)],
        out_specs=pl.BlockSpec(memory_space=pltpu.VMEM),
        out_shape=jax.ShapeDtypeStruct((N_HEADS, N_BUCKETS, 1, 1), F32),
        scratch_shapes=[pltpu.VMEM((N_HEADS, N_BUCKETS, 1, 2 * SPAN), F32)],
    )(dbias, buckets)
    return out.reshape(N_HEADS, N_BUCKETS).T


def _dot_nt(a, b):
    return lax.dot_general(a, b, (((1,), (1,)), ((), ())), preferred_element_type=F32)


def _dot_nn(a, b):
    return lax.dot_general(a, b, (((1,), (0,)), ((), ())), preferred_element_type=F32)


def _dot_tn(a, b):
    return lax.dot_general(a, b, (((0,), (0,)), ((), ())), preferred_element_type=F32)


def _band_masks(n, nb):
    qi = lax.broadcasted_iota(jnp.int32, (SPAN, SPAN), 0)
    ki = lax.broadcasted_iota(jnp.int32, (SPAN, SPAN), 1)
    prev_ok = jnp.logical_and(ki >= qi, n > 0)
    cur_ok = ki <= qi
    next_ok = jnp.logical_and(ki >= qi, n < nb - 1)
    return prev_ok, cur_ok, next_ok


def _wide_band_mask(n):
    qi = lax.broadcasted_iota(jnp.int32, (SPAN, 2 * SPAN), 0)
    ki = lax.broadcasted_iota(jnp.int32, (SPAN, 2 * SPAN), 1)
    prev_ok = jnp.logical_and(jnp.logical_and(ki < SPAN, ki >= qi), n > 0)
    cur_ok = jnp.logical_and(ki >= SPAN, ki - SPAN <= qi)
    return jnp.logical_or(prev_ok, cur_ok)


def _attn_plan(S, group):
    r = DILATED_PATTERNS[group][1]
    hp, per = (HEADS_PER_GROUP, 1) if r == 1 else (2, 4)
    return r, S // (r * SPAN), hp, per


def _residue_rows(rho, r):
    return slice(None) if r == 1 else pl.ds(rho, SPAN, stride=r)


def _for_residues(r, per, fn):
    if r == per:
        for u in range(per):
            fn(u)
        return

    def step(i, carry):
        for u in range(per):
            fn(i * per + u)
        return carry

    lax.fori_loop(0, r // per, step, 0)


def _attn_fwd_call(proj, bias, group):
    S = proj.shape[0]
    r, nb, hp, per = _attn_plan(S, group)
    scale = HEAD_DIM ** -0.5
    kinds = ("q", "kp", "kc", "vp", "vc") if nb > 1 else ("q", "kc", "vc")

    per_kind = _refs_per_kind(r, hp)

    def body(*refs):
        ins = {kind: refs[i * per_kind:(i + 1) * per_kind] for i, kind in enumerate(kinds)}
        b_ref, o_ref, lse_ref = refs[len(kinds) * per_kind:]
        n = pl.program_id(1)
        prev_ok, cur_ok, _ = _band_masks(n, nb)

        band_ok = _wide_band_mask(n) if nb > 1 else cur_ok

        def residue(rho):
            rows = _residue_rows(rho, r)
            for j in range(hp):
                get = lambda kind: _head_rows(ins[kind], j, rows, r).astype(BF16)
                q = get("q")
                if nb > 1:
                    keys, vals, bias_j = jnp.concatenate([get("kp"), get("kc")], axis=0), jnp.concatenate([get("vp"), get("vc")], axis=0), b_ref[j]
                else:
                    keys, vals, bias_j = get("kc"), get("vc"), b_ref[j, :, SPAN:]
                s = jnp.where(band_ok, _dot_nt(q, keys) * scale + bias_j, NEG_INF)
                m = jnp.max(s, axis=-1, keepdims=True)
                p = jnp.exp(s - m)
                den = jnp.sum(p, axis=-1, keepdims=True)
                o_ref[j, rows, :] = _dot_nn(p.astype(BF16), vals) / den
                lse_ref[j, rows, :] = jnp.broadcast_to(m + jnp.log(den), (SPAN, HEAD_DIM))

        _for_residues(r, per, residue)

    in_specs = [_head_spec(r, nb, hp, kind, group, jj) for kind in kinds for jj in range(per_kind)]
    in_specs.append(pl.BlockSpec((hp, SPAN, 2 * SPAN), lambda j, n: (group * (HEADS_PER_GROUP // hp) + j, 0, 0)))
    out = pl.BlockSpec((hp, r * SPAN, HEAD_DIM), lambda j, n: (j, n, 0))
    return pl.pallas_call(
        body, name=f"attn_fwd_g{group}", grid=(HEADS_PER_GROUP // hp, nb),
        in_specs=in_specs, out_specs=[out] * 2,
        out_shape=[jax.ShapeDtypeStruct((HEADS_PER_GROUP, S, HEAD_DIM), F32)] * 2,
        compiler_params=_params(("parallel", "parallel"), VMEM_LIMIT),
    )(*([proj] * (len(in_specs) - 1)), bias)


_PROJ_PART = dict(q=0, qn=0, kp=1, kc=1, vp=2, vc=2)


def _refs_per_kind(r, hp):
    return 1 if r == 1 else hp


def _head_rows(refs, j, rows, r):
    return refs[0][:, j * HEAD_DIM:(j + 1) * HEAD_DIM] if r == 1 else refs[j][rows, :]


def _head_spec(r, nb, hp, kind, group, jj):
    if kind in _PROJ_PART:
        base = (_PROJ_PART[kind] * N_GROUPS + group) * HEADS_PER_GROUP
    else:
        base = 0
    if kind.endswith("p"):
        row = lambda n: jnp.maximum(n - 1, 0)
    elif kind.endswith("n"):
        row = lambda n: jnp.minimum(n + 1, nb - 1)
    else:
        row = lambda n: n
    if r == 1:
        return pl.BlockSpec((SPAN, hp * HEAD_DIM), lambda j, n: (row(n), base // hp + j))
    return pl.BlockSpec((r * SPAN, HEAD_DIM), lambda j, n: (row(n), base + j * hp + jj))


def _attn_merge_call(parts):
    S = parts[0].shape[1]

    def body(o1, s1, o2, s2, o3, s3, a_ref, ab_ref, lse_ref):
        for j in range(HEADS_PER_GROUP):
            sl = slice(j * HEAD_DIM, (j + 1) * HEAD_DIM)
            mx = jnp.maximum(jnp.maximum(s1[j], s2[j]), s3[j])
            w1 = jnp.exp(s1[j] - mx)
            w2 = jnp.exp(s2[j] - mx)
            w3 = jnp.exp(s3[j] - mx)
            den = w1 + w2 + w3
            a = (w1 * o1[j] + w2 * o2[j] + w3 * o3[j]) / den
            a_ref[:, sl] = a
            ab_ref[:, sl] = a.astype(BF16)
            lse_ref[:, sl] = mx + jnp.log(den)

    heads = pl.BlockSpec((HEADS_PER_GROUP, ROW_TILE, HEAD_DIM), lambda i: (0, i, 0))
    return pl.pallas_call(
        body, name="attn_merge", grid=(S // ROW_TILE,),
        in_specs=[heads] * 6, out_specs=[_row_spec(GROUP_WIDTH)] * 3,
        out_shape=[jax.ShapeDtypeStruct((S, GROUP_WIDTH), F32), jax.ShapeDtypeStruct((S, GROUP_WIDTH), BF16),
                   jax.ShapeDtypeStruct((S, GROUP_WIDTH), F32)],
        compiler_params=_params(("parallel",)),
    )(*parts)


def _attn_delta_call(a, da):
    S = a.shape[0]

    def body(a_ref, da_ref, d_ref):
        for j in range(HEADS_PER_GROUP):
            sl = slice(j * HEAD_DIM, (j + 1) * HEAD_DIM)
            d = jnp.sum(a_ref[:, sl] * da_ref[:, sl], axis=-1, keepdims=True)
            d_ref[:, sl] = jnp.broadcast_to(d, (ROW_TILE, HEAD_DIM))

    return pl.pallas_call(
        body, name="attn_delta", grid=(S // ROW_TILE,),
        in_specs=[_row_spec(GROUP_WIDTH)] * 2, out_specs=_row_spec(GROUP_WIDTH),
        out_shape=jax.ShapeDtypeStruct((S, GROUP_WIDTH), F32),
        compiler_params=_params(("parallel",)),
    )(a, da)


def _attn_bwd_call(proj, bias, da, lse, delta, group):
    S = proj.shape[0]
    r, nb, hp, per = _attn_plan(S, group)
    scale = HEAD_DIM ** -0.5
    kinds = ("q", "qn", "kp", "kc", "vp", "vc", "da", "dan", "lse", "lsen", "dl", "dln") if nb > 1 else ("q", "kc", "vc", "da", "lse", "dl")
    source = dict(da=da, dan=da, lse=lse, lsen=lse, dl=delta, dln=delta)

    per_kind = _refs_per_kind(r, hp)

    def body(*refs):
        ins = {kind: refs[i * per_kind:(i + 1) * per_kind] for i, kind in enumerate(kinds)}
        b_ref, dq_ref, dk_ref, dv_ref, db_ref = refs[len(kinds) * per_kind:]
        n = pl.program_id(1)
        prev_ok, cur_ok, next_ok = _band_masks(n, nb)

        @pl.when(n == 0)
        def _():
            db_ref[...] = jnp.zeros_like(db_ref)

        band_ok = _wide_band_mask(n) if nb > 1 else cur_ok

        def residue(rho):
            rows = _residue_rows(rho, r)
            for j in range(hp):
                get = lambda kind: _head_rows(ins[kind], j, rows, r)
                q = get("q").astype(BF16)
                kc = get("kc").astype(BF16)
                vc = get("vc").astype(BF16)
                dav = get("da").astype(BF16)
                lse_q, dl_q = get("lse"), get("dl")
                if nb == 1:
                    pc = jnp.exp(jnp.where(cur_ok, _dot_nt(q, kc) * scale + b_ref[j, :, SPAN:], NEG_INF) - lse_q)
                    dsc = pc * (_dot_nt(dav, vc) - dl_q)
                    dsc_b = dsc.astype(BF16)
                    dq = _dot_nn(dsc_b, kc)
                    dk = _dot_tn(dsc_b, q)
                    dv = _dot_tn(pc.astype(BF16), dav)
                    db_ref[j, :, SPAN:] += dsc
                else:
                    qn = get("qn").astype(BF16)
                    dan = get("dan").astype(BF16)
                    keys = jnp.concatenate([get("kp").astype(BF16), kc], axis=0)
                    vals = jnp.concatenate([get("vp").astype(BF16), vc], axis=0)
                    wide = lambda t: jnp.concatenate([t, t], axis=1)
                    p = jnp.exp(jnp.where(band_ok, _dot_nt(q, keys) * scale + b_ref[j], NEG_INF) - wide(lse_q))
                    ds = p * (_dot_nt(dav, vals) - wide(dl_q))
                    dq = _dot_nn(ds.astype(BF16), keys)
                    db_ref[j] += ds
                    pn = jnp.exp(jnp.where(next_ok, _dot_nt(qn, kc) * scale + b_ref[j, :, :SPAN], NEG_INF) - get("lsen"))
                    dsn = pn * (_dot_nt(dan, vc) - get("dln"))
                    both = lambda cur_part, next_part: jnp.concatenate([cur_part.astype(BF16), next_part.astype(BF16)], axis=0)
                    dk = _dot_tn(both(ds[:, SPAN:], dsn), jnp.concatenate([q, qn], axis=0))
                    dv = _dot_tn(both(p[:, SPAN:], pn), jnp.concatenate([dav, dan], axis=0))
                dq_ref[j, rows, :] = dq * scale
                dk_ref[j, rows, :] = dk * scale
                dv_ref[j, rows, :] = dv

        _for_residues(r, per, residue)

    per_group = HEADS_PER_GROUP // hp
    band = (hp, SPAN, 2 * SPAN)
    in_specs = [_head_spec(r, nb, hp, kind, group, jj) for kind in kinds for jj in range(per_kind)]
    in_specs.append(pl.BlockSpec(band, lambda j, n: (group * per_group + j, 0, 0)))
    operands = [source.get(kind, proj) for kind in kinds for _ in range(per_kind)] + [bias]
    out = pl.BlockSpec((hp, r * SPAN, HEAD_DIM), lambda j, n: (j, n, 0))
    return pl.pallas_call(
        body, name=f"attn_bwd_g{group}", grid=(per_group, nb),
        in_specs=in_specs,
        out_specs=[out] * 3 + [pl.BlockSpec(band, lambda j, n: (j, 0, 0))],
        out_shape=[jax.ShapeDtypeStruct((HEADS_PER_GROUP, S, HEAD_DIM), F32)] * 3
        + [jax.ShapeDtypeStruct((HEADS_PER_GROUP, SPAN, 2 * SPAN), F32)],
        compiler_params=_params(("parallel", "arbitrary"), VMEM_LIMIT),
    )(*operands)


def _dproj_call(dqkv, tails):
    S = tails[0].shape[0]
    width = len(dqkv) * GROUP_WIDTH + sum(t.shape[1] for t in tails)

    def body(*refs):
        o_ref = refs[-1]
        col = 0
        for ref in refs[:len(dqkv)]:
            for j in range(HEADS_PER_GROUP):
                o_ref[:, col:col + HEAD_DIM] = ref[j].astype(BF16)
                col += HEAD_DIM
        for ref in refs[len(dqkv):-1]:
            o_ref[:, col:col + ref.shape[1]] = ref[...]
            col += ref.shape[1]

    heads = pl.BlockSpec((HEADS_PER_GROUP, ROW_TILE, HEAD_DIM), lambda i: (0, i, 0))
    return pl.pallas_call(
        body, name="dproj_assemble", grid=(S // ROW_TILE,),
        in_specs=[heads] * len(dqkv) + [_row_spec(t.shape[1]) for t in tails],
        out_specs=_row_spec(width), out_shape=jax.ShapeDtypeStruct((S, width), BF16),
        compiler_params=_params(("parallel",)),
    )(*dqkv, *tails)


def _tap_rows(xpad_ref, t0, k, width, pad):
    return xpad_ref[pl.ds(t0 + (pad - (width - 1 - k)), TIME_BLOCK), :]


def _conv_block(xpad_ref, t0, w_ref, width, pad):
    acc = None
    for k in range(width):
        term = w_ref[k:k + 1, :] * _tap_rows(xpad_ref, t0, k, width, pad)
        acc = term if acc is None else acc + term
    return acc


def _conv_transpose_block(dpad_ref, t0, w_ref, width):
    acc = None
    for k in range(width):
        term = w_ref[k:k + 1, :] * dpad_ref[pl.ds(t0 + (width - 1 - k), TIME_BLOCK), :]
        acc = term if acc is None else acc + term
    return acc


def _conv_weight_grad(xpad_ref, t0, dy, dw_ref, width, pad):
    for k in range(width):
        dw_ref[k:k + 1, :] += jnp.sum(dy * _tap_rows(xpad_ref, t0, k, width, pad), axis=0, keepdims=True)


def _time_loop(S, step, skip_first=0, skip_last=0):
    def it(tb, carry):
        step(pl.multiple_of(tb * TIME_BLOCK, TIME_BLOCK))
        return carry

    lax.fori_loop(skip_first, S // TIME_BLOCK - skip_last, it, 0)


def _fill_head(head_ref, x_ref, pad):
    head_ref[0:pad, :] = jnp.zeros((pad, LANES), F32)
    head_ref[pad:, :] = x_ref[0:TIME_BLOCK, :]


def _fill_tail(tail_ref, x_ref, pad):
    S = x_ref.shape[0]
    tail_ref[0:TIME_BLOCK, :] = x_ref[S - TIME_BLOCK:S, :]
    tail_ref[TIME_BLOCK:, :] = jnp.zeros((pad, LANES), F32)


def _conv_fwd_call(proj, col0, w, b):
    S = proj.shape[0]
    C = w.shape[1]
    nt = C // LANES
    v0, g0 = col0 // LANES, (col0 + C) // LANES

    def body(val_ref, gate_ref, w_ref, b_ref, o_ref, pad_ref):
        pad_ref[0:CONV_PAD, :] = jnp.zeros((CONV_PAD, LANES), F32)
        pad_ref[CONV_PAD:, :] = val_ref[...] * _sigmoid(gate_ref[...])

        def step(t0):
            o_ref[pl.ds(t0, TIME_BLOCK), :] = _conv_block(pad_ref, t0, w_ref, CONV_WIDTH, CONV_PAD) + b_ref[...]

        _time_loop(S, step)

    seq = lambda off: pl.BlockSpec((S, LANES), lambda i: (0, off + i))
    return pl.pallas_call(
        body, name="conv_module", grid=(nt,),
        in_specs=[seq(v0), seq(g0), pl.BlockSpec((CONV_WIDTH, LANES), lambda i: (0, i)), pl.BlockSpec((1, LANES), lambda i: (0, i))],
        out_specs=seq(0), out_shape=jax.ShapeDtypeStruct((S, C), F32),
        scratch_shapes=[pltpu.VMEM((S + CONV_PAD, LANES), F32)],
        compiler_params=_params(("parallel",)),
    )(proj, proj, w, b)


def _conv_bwd_call(proj, col0, w, dc1):
    S = proj.shape[0]
    C = w.shape[1]
    nt = C // LANES
    v0, g0 = col0 // LANES, (col0 + C) // LANES

    def body(val_ref, gate_ref, w_ref, dy_ref, dval_ref, dgate_ref, dw_ref, db_ref, xpad_ref, tail_ref, dwacc_ref):
        xpad_ref[0:CONV_PAD, :] = jnp.zeros((CONV_PAD, LANES), F32)
        xpad_ref[CONV_PAD:, :] = val_ref[...] * _sigmoid(gate_ref[...])
        _fill_tail(tail_ref, dy_ref, CONV_PAD)
        dwacc_ref[...] = jnp.zeros_like(dwacc_ref)

        def block(t0, dy_src, dy_t0):
            rows = pl.ds(t0, TIME_BLOCK)
            _conv_weight_grad(xpad_ref, t0, dy_ref[rows, :], dwacc_ref, CONV_WIDTH, CONV_PAD)
            dc0 = _conv_transpose_block(dy_src, dy_t0, w_ref, CONV_WIDTH)
            sg = _sigmoid(gate_ref[rows, :])
            dval_ref[rows, :] = (dc0 * sg).astype(BF16)
            dgate_ref[rows, :] = (dc0 * val_ref[rows, :] * sg * (1.0 - sg)).astype(BF16)

        _time_loop(S, lambda t0: block(t0, dy_ref, t0), skip_last=1)
        block(S - TIME_BLOCK, tail_ref, 0)
        dw_ref[...] = dwacc_ref[...]
        db_ref[...] = jnp.sum(dy_ref[...], axis=0, keepdims=True)

    seq = lambda off: pl.BlockSpec((S, LANES), lambda i: (0, off + i))
    return pl.pallas_call(
        body, name="conv_module_bwd", grid=(nt,),
        in_specs=[seq(v0), seq(g0), pl.BlockSpec((CONV_WIDTH, LANES), lambda i: (0, i)), seq(0)],
        out_specs=[seq(0), seq(0), pl.BlockSpec((CONV_PAD, LANES), lambda i: (0, i)), pl.BlockSpec((1, LANES), lambda i: (0, i))],
        out_shape=[jax.ShapeDtypeStruct((S, C), BF16), jax.ShapeDtypeStruct((S, C), BF16),
                   jax.ShapeDtypeStruct((CONV_PAD, C), F32), jax.ShapeDtypeStruct((1, C), F32)],
        scratch_shapes=[pltpu.VMEM((S + CONV_PAD, LANES), F32), pltpu.VMEM((TIME_BLOCK + CONV_PAD, LANES), F32),
                        pltpu.VMEM((CONV_PAD, LANES), F32)],
        compiler_params=_params(("parallel",)),
    )(proj, proj, w, dc1)


def _ffn_fwd_call(u, w, b):
    S, C2 = u.shape
    C = C2 // 2
    nt = C // LANES

    def body(ug_ref, uv_ref, wg_ref, wv_ref, bg_ref, bv_ref, f_ref, hg_ref, hv_ref):
        _fill_head(hg_ref, ug_ref, FFN_PAD)
        _fill_head(hv_ref, uv_ref, FFN_PAD)

        def block(t0, xg_ref, xv_ref, x_t0, pad):
            cg = _conv_block(xg_ref, x_t0, wg_ref, FFN_CONV_WIDTH, pad) + bg_ref[...]
            cv = _conv_block(xv_ref, x_t0, wv_ref, FFN_CONV_WIDTH, pad) + bv_ref[...]
            f_ref[pl.ds(t0, TIME_BLOCK), :] = (_gelu(cg) * cv).astype(BF16)

        block(0, hg_ref, hv_ref, 0, FFN_PAD)
        _time_loop(S, lambda t0: block(t0, ug_ref, uv_ref, t0, 0), skip_first=1)

    seq = lambda off: pl.BlockSpec((S, LANES), lambda i: (0, off + i))
    wsp = lambda off: pl.BlockSpec((FFN_CONV_WIDTH, LANES), lambda i: (0, off + i))
    bsp = lambda off: pl.BlockSpec((1, LANES), lambda i: (0, off + i))
    return pl.pallas_call(
        body, name="ffn_conv_geglu", grid=(nt,),
        in_specs=[seq(0), seq(nt), wsp(0), wsp(nt), bsp(0), bsp(nt)],
        out_specs=seq(0), out_shape=jax.ShapeDtypeStruct((S, C), BF16),
        scratch_shapes=[pltpu.VMEM((FFN_PAD + TIME_BLOCK, LANES), F32)] * 2,
        compiler_params=_params(("parallel",)),
    )(u, u, w, w, b, b)


def _ffn_bwd_call(u, w, b, df):
    S, C2 = u.shape
    C = C2 // 2
    nt = C // LANES

    def body(ug_ref, uv_ref, wg_ref, wv_ref, bg_ref, bv_ref, df_ref,
             du_ref, dwg_ref, dwv_ref, dbg_ref, dbv_ref,
             hg_ref, hv_ref, dg_ref, dv_ref, dwg_acc, dwv_acc, dbg_acc, dbv_acc):
        zeros = jnp.zeros((FFN_PAD, LANES), F32)
        _fill_head(hg_ref, ug_ref, FFN_PAD)
        _fill_head(hv_ref, uv_ref, FFN_PAD)
        dg_ref[S:, :] = zeros
        dv_ref[S:, :] = zeros
        dwg_acc[...] = jnp.zeros_like(dwg_acc)
        dwv_acc[...] = jnp.zeros_like(dwv_acc)
        dbg_acc[...] = jnp.zeros_like(dbg_acc)
        dbv_acc[...] = jnp.zeros_like(dbv_acc)

        def first(t0, xg_ref, xv_ref, x_t0, pad):
            rows = pl.ds(t0, TIME_BLOCK)
            cg = _conv_block(xg_ref, x_t0, wg_ref, FFN_CONV_WIDTH, pad) + bg_ref[...]
            cv = _conv_block(xv_ref, x_t0, wv_ref, FFN_CONV_WIDTH, pad) + bv_ref[...]
            dfb = df_ref[rows, :]
            gelu, gelu_grad = _gelu_and_grad(cg)
            dcg = dfb * cv * gelu_grad
            dcv = dfb * gelu
            dg_ref[rows, :] = dcg
            dv_ref[rows, :] = dcv
            _conv_weight_grad(xg_ref, x_t0, dcg, dwg_acc, FFN_CONV_WIDTH, pad)
            _conv_weight_grad(xv_ref, x_t0, dcv, dwv_acc, FFN_CONV_WIDTH, pad)
            dbg_acc[...] += jnp.sum(dcg, axis=0, keepdims=True)
            dbv_acc[...] += jnp.sum(dcv, axis=0, keepdims=True)

        def second(t0):
            rows = pl.ds(t0, TIME_BLOCK)
            du_ref[0, rows, :] = _conv_transpose_block(dg_ref, t0, wg_ref, FFN_CONV_WIDTH).astype(BF16)
            du_ref[1, rows, :] = _conv_transpose_block(dv_ref, t0, wv_ref, FFN_CONV_WIDTH).astype(BF16)

        first(0, hg_ref, hv_ref, 0, FFN_PAD)
        _time_loop(S, lambda t0: first(t0, ug_ref, uv_ref, t0, 0), skip_first=1)
        _time_loop(S, second)
        dwg_ref[...] = dwg_acc[...]
        dwv_ref[...] = dwv_acc[...]
        dbg_ref[...] = dbg_acc[...]
        dbv_ref[...] = dbv_acc[...]

    seq = lambda off: pl.BlockSpec((S, LANES), lambda i: (0, off + i))
    wsp = lambda off: pl.BlockSpec((FFN_CONV_WIDTH, LANES), lambda i: (0, off + i))
    bsp = lambda off: pl.BlockSpec((1, LANES), lambda i: (0, off + i))
    return pl.pallas_call(
        body, name="ffn_conv_geglu_bwd", grid=(nt,),
        in_specs=[seq(0), seq(nt), wsp(0), wsp(nt), bsp(0), bsp(nt), seq(0)],
        out_specs=[pl.BlockSpec((2, S, LANES), lambda i: (0, 0, i)),
                   pl.BlockSpec((SUBLANES, LANES), lambda i: (0, i)), pl.BlockSpec((SUBLANES, LANES), lambda i: (0, i)),
                   bsp(0), bsp(0)],
        out_shape=[jax.ShapeDtypeStruct((2, S, C), BF16)] + [jax.ShapeDtypeStruct((SUBLANES, C), F32)] * 2
        + [jax.ShapeDtypeStruct((1, C), F32)] * 2,
        scratch_shapes=[pltpu.VMEM((FFN_PAD + TIME_BLOCK, LANES), F32)] * 2 + [pltpu.VMEM((S + FFN_PAD, LANES), F32)] * 2
        + [pltpu.VMEM((SUBLANES, LANES), F32)] * 2
        + [pltpu.VMEM((1, LANES), F32)] * 2,
        compiler_params=_params(("parallel",)),
    )(u, u, w, w, b, b, df)


def _adamw(w_ref, g_ref, m_ref, v_ref, d_ref, mo_ref, vo_ref):
    gv = g_ref[...]
    mn = ADAM_B1 * m_ref[...] + (1.0 - ADAM_B1) * gv
    vn = ADAM_B2 * v_ref[...] + (1.0 - ADAM_B2) * (gv * gv)
    mo_ref[...] = mn
    vo_ref[...] = vn
    m_hat = mn * (1.0 / (1.0 - ADAM_B1 ** ADAM_STEP))
    v_hat = vn * (1.0 / (1.0 - ADAM_B2 ** ADAM_STEP))
    d_ref[...] = -ADAM_LR * (m_hat / (jnp.sqrt(v_hat) + ADAM_EPS) + ADAM_WD * w_ref[...])


def _adamw_call(w, g, m, v, name):
    R, C = w.shape
    tr = _row_tile(R, C)

    def body(w_ref, g_ref, m_ref, v_ref, go_ref, d_ref, mo_ref, vo_ref):
        go_ref[...] = g_ref[...]
        _adamw(w_ref, g_ref, m_ref, v_ref, d_ref, mo_ref, vo_ref)

    spec = pl.BlockSpec((tr, C), lambda i: (i, 0))
    return pl.pallas_call(
        body, name=name, grid=(R // tr,),
        in_specs=[spec] * 4, out_specs=[spec] * 4,
        out_shape=[jax.ShapeDtypeStruct((R, C), F32)] * 4,
        compiler_params=_params(("parallel",)),
    )(w, g, m, v)


def _adamw_small_call(ws, gs, ms, vs):
    n = len(ws)

    def body(*refs):
        w_refs, g_refs, m_refs, v_refs, d_refs, mo_refs, vo_refs = (refs[i * n:(i + 1) * n] for i in range(7))
        for i in range(n):
            _adamw(w_refs[i], g_refs[i], m_refs[i], v_refs[i], d_refs[i], mo_refs[i], vo_refs[i])

    whole = pl.BlockSpec(memory_space=pltpu.VMEM)
    outs = pl.pallas_call(
        body, name="adamw_small",
        in_specs=[whole] * (4 * n), out_specs=[whole] * (3 * n),
        out_shape=[jax.ShapeDtypeStruct(w.shape, F32) for w in ws] * 3,
    )(*ws, *gs, *ms, *vs)
    return outs[:n], outs[n:2 * n], outs[2 * n:]


def _position():
    return lax.axis_index("x"), lax.axis_index("y"), lax.axis_index("c")


def _chip_peers(x, y):
    return [(x, 1 - y), (1 - x, y), (1 - x, 1 - y)]


def _half_rows(ref, core, rows):
    h = rows // 2
    start = pl.multiple_of(core * h, 16)
    return ref.at[pl.ds(start, h), :] if len(ref.shape) == 2 else ref.at[:, pl.ds(start, h), :]


def _shard_half(ref, shard, core, rows):
    h = rows // 2
    return ref.at[shard, pl.ds(pl.multiple_of(core * h, 16), h), :]


ANY = pl.BlockSpec(memory_space=pl.ANY)


def _first_hop_copies(srcs, lands):
    x, y, c = _position()
    chip = 2 * x + y
    targets = [(px, py, c) for px, py in _chip_peers(x, y)] + [(x, y, 1 - c)]
    rows = srcs[0].shape[0]
    out = []
    for i, (s, l) in enumerate(zip(srcs, lands)):
        for k, dev in enumerate(targets):
            if i == 0 and k < 3:
                out.append((_half_rows(s, c, rows), _shard_half(l, chip, c, rows), dev, k))
            else:
                out.append((s, l.at[chip], dev, len(targets) * i + k))
    return out


def _second_hop_copies(srcs, lands):
    x, y, c = _position()
    rows = lands[0].shape[1]
    out = []
    for k, (px, py) in enumerate(_chip_peers(x, y)):
        half = _shard_half(lands[0], 2 * px + py, c, rows)
        out.append((half, half, (x, y, 1 - c), k))
    return out


HBM_SPEC = pl.BlockSpec(memory_space=pltpu.HBM)
SEM_SPEC = pl.BlockSpec(memory_space=pltpu.SEMAPHORE)
DATAFLOW = pltpu.SideEffectType.DATAFLOW_SIDE_EFFECTING


def _in_hbm(a):
    return pltpu.with_memory_space_constraint(a, pltpu.HBM)


def _split_start(name, groups, after, carry=None):
    spans, arrays = [], []
    for srcs, lands, _, _ in groups:
        spans.append((len(arrays), len(srcs), len(lands)))
        arrays += list(srcs) + list(lands)
    if carry is not None:
        arrays.append(carry)
    na, ng = len(arrays), len(groups)

    def body(*refs):
        sems, token = refs[na + 1:na + 1 + 2 * ng], refs[-1]
        for g, (_, _, _, copies) in enumerate(groups):
            off, ns, nl = spans[g]
            for src, dst, dev, idx in copies(refs[off:off + ns], refs[off + ns:off + ns + nl]):
                pltpu.make_async_remote_copy(src_ref=src, dst_ref=dst, send_sem=sems[2 * g].at[idx], recv_sem=sems[2 * g + 1].at[idx],
                                             device_id=dev, device_id_type=MESH).start()
        token[...] = jnp.zeros_like(token)

    outs = pl.pallas_call(
        body, name=name,
        in_specs=[HBM_SPEC] * na + [ANY],
        out_specs=[SEM_SPEC] * (2 * ng) + [HBM_SPEC] * na + [pl.BlockSpec(memory_space=pltpu.VMEM)],
        out_shape=[pltpu.SemaphoreType.DMA((n_sems,)) for _, _, n_sems, _ in groups for _ in range(2)]
        + [pltpu.HBM(a.shape, a.dtype) for a in arrays] + [jax.ShapeDtypeStruct((SUBLANES, LANES), F32)],
        input_output_aliases={i: 2 * ng + i for i in range(na)},
        compiler_params=pltpu.CompilerParams(has_side_effects=DATAFLOW),
    )(*[_in_hbm(a) for a in arrays], after)
    started = []
    for g, (off, ns, nl) in enumerate(spans):
        thru = outs[2 * ng + off:2 * ng + off + ns + nl]
        started.append(dict(send=outs[2 * g], recv=outs[2 * g + 1], srcs=list(thru[:ns]), lands=list(thru[ns:]),
                            tile=outs[-1], token=outs[-1][0, 0], carry=None if carry is None else outs[2 * ng + na - 1]))
    return started


def _split_wait(name, started, copies, after):
    n, m = len(started["srcs"]), len(started["lands"])

    def body(*refs):
        src_refs, land_refs = refs[:n], refs[n:n + m]
        send_sem, recv_sem = refs[n + m], refs[n + m + 1]
        for src, dst, dev, idx in copies(src_refs, land_refs):
            cp = pltpu.make_async_remote_copy(src_ref=src, dst_ref=dst, send_sem=send_sem.at[idx], recv_sem=recv_sem.at[idx],
                                              device_id=dev, device_id_type=MESH)
            cp.wait_send()
            cp.wait_recv()

    arrays = started["srcs"] + started["lands"]
    outs = pl.pallas_call(
        body, name=name,
        in_specs=[HBM_SPEC] * (n + m) + [SEM_SPEC, SEM_SPEC, ANY],
        out_specs=[HBM_SPEC] * (n + m),
        out_shape=[pltpu.HBM(a.shape, a.dtype) for a in arrays],
        input_output_aliases={i: i for i in range(n + m)},
        compiler_params=pltpu.CompilerParams(has_side_effects=DATAFLOW),
    )(*arrays, started["send"], started["recv"], after)
    return list(outs)


def _gather_copies(srcs, lands):
    x, y, c = _position()
    chip = 2 * x + y
    targets = [(px, py, c) for px, py in _chip_peers(x, y)] + [(x, y, 1 - c)]
    return [(s, l.at[chip], dev, len(targets) * i + k) for i, (s, l) in enumerate(zip(srcs, lands)) for k, dev in enumerate(targets)]


def _sibling_copies(srcs, lands):
    x, y, c = _position()
    return [(_half_rows(srcs[0], 1 - c, srcs[0].shape[1]), lands[0], (x, y, 1 - c), 0)]


def _exchange_copies(srcs, lands):
    x, y, c = _position()
    return [(srcs[0].at[2 * px + py], lands[0].at[k], (px, py, c), k) for k, (px, py) in enumerate(_chip_peers(x, y))]


def _pair_sum_call(grad, recv, core, name):
    _, h, B = recv.shape
    tr = _row_tile(h, B)

    def body(core_ref, g_ref, r_ref, o_ref, ob_ref):
        s = g_ref[...] + r_ref[...]
        o_ref[...] = s
        ob_ref[...] = s.astype(BF16)

    g_spec = pl.BlockSpec((None, tr, B), lambda q, i, core_ref: (q, core_ref[0] * (h // tr) + i, 0))
    spec = pl.BlockSpec((None, tr, B), lambda q, i, core_ref: (q, i, 0))
    return pl.pallas_call(
        body, name=name,
        grid_spec=pltpu.PrefetchScalarGridSpec(num_scalar_prefetch=1, grid=(N_CHIPS, h // tr), in_specs=[g_spec, spec],
                                               out_specs=[spec, spec]),
        out_shape=[jax.ShapeDtypeStruct(recv.shape, F32), jax.ShapeDtypeStruct(recv.shape, BF16)],
        compiler_params=_params(("parallel", "parallel")),
    )(core, grad, recv)


def _chip_sum_call(partial, recv, chip_core, name):
    _, h, B = recv.shape
    tr = _row_tile(h, B)

    def body(cc_ref, p_ref, r_ref, o_ref):
        o_ref[...] = ((p_ref[...] + r_ref[0].astype(F32)) + r_ref[1].astype(F32)) + r_ref[2].astype(F32)

    return pl.pallas_call(
        body, name=name,
        grid_spec=pltpu.PrefetchScalarGridSpec(
            num_scalar_prefetch=1, grid=(h // tr,),
            in_specs=[pl.BlockSpec((None, tr, B), lambda i, cc_ref: (cc_ref[0], i, 0)),
                      pl.BlockSpec((3, tr, B), lambda i, cc_ref: (0, i, 0))],
            out_specs=pl.BlockSpec((tr, B), lambda i, cc_ref: (cc_ref[1] * (h // tr) + i, 0))),
        out_shape=jax.ShapeDtypeStruct((2 * h, B), F32),
        compiler_params=_params(("parallel",)),
    )(chip_core, partial, recv)


def _sibling_assemble_call(shards, name="grad_sibling_assemble"):
    n = len(shards)

    def body(*refs):
        ins, outs = refs[:n], refs[n:2 * n]
        send_sems, recv_sems = refs[2 * n:]
        x, y, c = _position()
        copies = []
        for i in range(n):
            rows = shards[i].shape[0]
            cp = pltpu.make_async_remote_copy(src_ref=_half_rows(ins[i], c, rows), dst_ref=_half_rows(outs[i], c, rows),
                                              send_sem=send_sems.at[i], recv_sem=recv_sems.at[i],
                                              device_id=(x, y, 1 - c), device_id_type=MESH)
            cp.start()
            copies.append(cp)
        for cp in copies:
            cp.wait()

    return pl.pallas_call(
        body, name=name,
        in_specs=[ANY] * n, out_specs=[ANY] * n,
        out_shape=[jax.ShapeDtypeStruct(s.shape, F32) for s in shards],
        input_output_aliases={i: i for i in range(n)},
        scratch_shapes=[pltpu.SemaphoreType.DMA((n,)), pltpu.SemaphoreType.DMA((n,))],
    )(*shards)


N_DEVICES = 8


def _allsum_copies(srcs, lands):
    x, y, c = _position()
    me = 4 * x + 2 * y + c
    out = []
    for k in range(1, N_DEVICES):
        peer = (1 - x if k & 4 else x, 1 - y if k & 2 else y, 1 - c if k & 1 else c)
        out.append((srcs[0], lands[0].at[me], peer, k - 1))
    return out


def _ordered_sum_call(mine, landed, me):
    rows = mine.shape[0]

    def body(me_ref, x_ref, l_ref, o_ref):
        acc = jnp.where(me_ref[0] == 0, x_ref[...], l_ref[0])
        for d in range(1, N_DEVICES):
            acc = acc + jnp.where(me_ref[0] == d, x_ref[...], l_ref[d])
        o_ref[...] = acc

    return pl.pallas_call(
        body, name="small_grad_sum",
        in_specs=[pl.BlockSpec(memory_space=pltpu.SMEM), pl.BlockSpec(memory_space=pltpu.VMEM), pl.BlockSpec(memory_space=pltpu.VMEM)],
        out_specs=pl.BlockSpec(memory_space=pltpu.VMEM),
        out_shape=jax.ShapeDtypeStruct((rows, LANES), F32),
    )(me, mine, landed)


def _pack(arrays):
    flat = jnp.concatenate([a.reshape(-1).astype(F32) for a in arrays])
    rows = -(-flat.shape[0] // LANES)
    rows = -(-rows // SUBLANES) * SUBLANES
    flat = jnp.pad(flat, (0, rows * LANES - flat.shape[0]))
    return flat.reshape(rows, LANES)


def _unpack(packed, shapes):
    flat = packed.reshape(-1)
    out, off = [], 0
    for shp in shapes:
        size = int(np.prod(shp))
        out.append(flat[off:off + size].reshape(shp))
        off += size
    return out


def _local_step(xs, target, P, late_weights, on_grad):
    S, D = xs.shape
    qkv_width = 3 * N_HEADS * HEAD_DIM
    glu_col0, gate_col0 = qkv_width, qkv_width + 2 * D
    shard_major = lambda g: g.reshape(N_CHIPS, g.shape[0] // N_CHIPS, g.shape[1])

    h1 = _rms_fwd_call(xs, P["norm_mix_pre"])
    buckets = _bucket_tables()
    bias = _bias_table_call(P["rel_bias"] + 0.0 * h1[0, 0].astype(F32), buckets)
    P = dict(P, **late_weights("in", bias))
    proj = _matmul(h1, P["w_in"], "nn", "proj_in")
    parts = []
    for g in range(N_GROUPS):
        parts += _attn_fwd_call(proj, bias, g)
    a, a_bf, lse = _attn_merge_call(parts)
    P = dict(P, **late_weights("mix", a_bf))
    y_a = _matmul(a_bf, P["w_attn_out"], "nn", "attn_out")
    c1 = _conv_fwd_call(proj, glu_col0, P["conv_dw_w"], P["conv_dw_b"])
    cact = _ln_silu_call(c1, P["conv_ln_g"], P["conv_ln_b"])
    y_c = _matmul(cact, P["conv_pw_w"], "nn", "conv_pw")
    mixed = _mix_call(proj, gate_col0, P["b_gate"], y_a, y_c)
    out = _matmul(mixed, P["w_out"], "nn", "mix_out")
    x1, h2 = _res1_call(xs, out, P["norm_mix_post"], P["norm_ffn_pre"])
    P = dict(P, **late_weights("up", h2))
    u = _matmul(h2, P["w_up"], "nn", "ffn_up")
    f = _ffn_fwd_call(u, P["ffn_conv_w"], P["ffn_conv_b"])
    P = dict(P, **late_weights("down", f))
    yff = _matmul(f, P["w_down"], "nn", "ffn_down")
    loss_tile, dx2, dyff, dg_ffn_post = _loss_call(yff, x1, P["norm_ffn_post"], target)

    G = {}
    G["norm_ffn_post"] = dg_ffn_post
    zero = on_grad("w_down", shard_major(_matmul(f, dyff, "tn", "ffn_down_dw")))
    df = _matmul(dyff, P["w_down"], "nt", "ffn_down_dx")
    du, dwg, dwv, dbg, dbv = _ffn_bwd_call(u, P["ffn_conv_w"], P["ffn_conv_b"] + zero, df)
    G["ffn_conv_w"] = jnp.concatenate([dwg[:FFN_CONV_WIDTH], dwv[:FFN_CONV_WIDTH]], axis=1)
    G["ffn_conv_b"] = jnp.concatenate([dbg, dbv], axis=1)
    zero = on_grad("w_up", _matmul(h2, du, "tn", "ffn_up_dw", out_shards=True))
    dh2 = _matmul(du, P["w_up"], "nt", "ffn_up_dx")
    dx1, dout, G["norm_ffn_pre"], G["norm_mix_post"] = _mid_bwd_call(x1, P["norm_ffn_pre"] + zero, dh2, dx2, out, P["norm_mix_post"])
    zero = on_grad("w_out", shard_major(_matmul(mixed, dout, "tn", "mix_out_dw")))
    dmixed = _matmul(dout, P["w_out"], "nt", "mix_out_dx")
    dya, dyc, dga, dgc, dba, dbc = _mix_bwd_call(dmixed, proj, gate_col0, P["b_gate"] + zero, y_a, y_c)
    G["b_gate"] = jnp.concatenate([dba, dbc], axis=1)
    zero = on_grad("w_attn_out", _matmul(a_bf, dya, "tn", "attn_out_dw", out_shards=True))
    zero = zero + on_grad("conv_pw_w", shard_major(_matmul(cact, dyc, "tn", "conv_pw_dw")))
    da = _matmul(dya, P["w_attn_out"], "nt", "attn_out_dx")
    dcact = _matmul(dyc, P["conv_pw_w"], "nt", "conv_pw_dx")
    dc1, G["conv_ln_g"], G["conv_ln_b"] = _ln_silu_bwd_call(c1, P["conv_ln_g"] + zero, P["conv_ln_b"], dcact)
    dval, dgate, dw_dw, G["conv_dw_b"] = _conv_bwd_call(proj, glu_col0, P["conv_dw_w"], dc1)
    G["conv_dw_w"] = dw_dw[:CONV_WIDTH]
    delta = _attn_delta_call(a, da)
    dqs, dks, dvs, dbs = [], [], [], []
    for g in range(N_GROUPS):
        dq, dk, dv, db = _attn_bwd_call(proj, bias, da, lse, delta, g)
        dqs.append(dq)
        dks.append(dk)
        dvs.append(dv)
        dbs.append(db)
    G["rel_bias"] = _bias_grad_call(jnp.concatenate(dbs, axis=0), buckets)
    dproj = _dproj_call(dqs + dks + dvs, [dval, dgate, dga, dgc])
    zero, dproj = on_grad("w_in", _matmul(h1, dproj, "tn", "proj_in_dw", out_shards=True), carry=dproj)
    dh1 = _matmul(dproj, P["w_in"], "nt", "proj_in_dx")
    zero = zero + on_grad(None, dh1)
    grad_x, G["norm_mix_pre"] = _in_bwd_call(xs, P["norm_mix_pre"] + zero, dh1, dx1)
    return loss_tile, grad_x, G


def kernel(x, w_in, b_gate, rel_bias, w_attn_out, conv_dw_w, conv_dw_b, conv_ln_g, conv_ln_b, conv_pw_w, w_out, norm_mix_pre, norm_mix_post, norm_ffn_pre, norm_ffn_post, w_up, ffn_conv_w, ffn_conv_b, w_down, loss_target, m_w_in, m_b_gate, m_rel_bias, m_w_attn_out, m_conv_dw_w, m_conv_dw_b, m_conv_ln_g, m_conv_ln_b, m_conv_pw_w, m_w_out, m_norm_mix_pre, m_norm_mix_post, m_norm_ffn_pre, m_norm_ffn_post, m_w_up, m_ffn_conv_w, m_ffn_conv_b, m_w_down, v_w_in, v_b_gate, v_rel_bias, v_w_attn_out, v_conv_dw_w, v_conv_dw_b, v_conv_ln_g, v_conv_ln_b, v_conv_pw_w, v_w_out, v_norm_mix_pre, v_norm_mix_post, v_norm_ffn_pre, v_norm_ffn_post, v_w_up, v_ffn_conv_w, v_ffn_conv_b, v_w_down):
    weights = dict(w_in=w_in, b_gate=b_gate, rel_bias=rel_bias, w_attn_out=w_attn_out, conv_dw_w=conv_dw_w, conv_dw_b=conv_dw_b,
                   conv_ln_g=conv_ln_g, conv_ln_b=conv_ln_b, conv_pw_w=conv_pw_w, w_out=w_out, norm_mix_pre=norm_mix_pre,
                   norm_mix_post=norm_mix_post, norm_ffn_pre=norm_ffn_pre, norm_ffn_post=norm_ffn_post, w_up=w_up,
                   ffn_conv_w=ffn_conv_w, ffn_conv_b=ffn_conv_b, w_down=w_down)
    m_in = dict(w_in=m_w_in, b_gate=m_b_gate, rel_bias=m_rel_bias, w_attn_out=m_w_attn_out, conv_dw_w=m_conv_dw_w,
                conv_dw_b=m_conv_dw_b, conv_ln_g=m_conv_ln_g, conv_ln_b=m_conv_ln_b, conv_pw_w=m_conv_pw_w, w_out=m_w_out,
                norm_mix_pre=m_norm_mix_pre, norm_mix_post=m_norm_mix_post, norm_ffn_pre=m_norm_ffn_pre,
                norm_ffn_post=m_norm_ffn_post, w_up=m_w_up, ffn_conv_w=m_ffn_conv_w, ffn_conv_b=m_ffn_conv_b, w_down=m_w_down)
    v_in = dict(w_in=v_w_in, b_gate=v_b_gate, rel_bias=v_rel_bias, w_attn_out=v_w_attn_out, conv_dw_w=v_conv_dw_w,
                conv_dw_b=v_conv_dw_b, conv_ln_g=v_conv_ln_g, conv_ln_b=v_conv_ln_b, conv_pw_w=v_conv_pw_w, w_out=v_w_out,
                norm_mix_pre=v_norm_mix_pre, norm_mix_post=v_norm_mix_post, norm_ffn_pre=v_norm_ffn_pre,
                norm_ffn_post=v_norm_ffn_post, w_up=v_w_up, ffn_conv_w=v_ffn_conv_w, ffn_conv_b=v_ffn_conv_b, w_down=v_w_down)
    names = list(weights)
    xi, yi, ci = _position()
    chip = 2 * xi + yi
    core_arr = jnp.reshape(ci, (1,)).astype(jnp.int32)

    xs = x[0]
    target = loss_target[0]
    S, D = xs.shape

    big = ["w_in", "w_attn_out", "conv_pw_w", "w_out", "w_up", "w_down"]
    row_sharded = ("conv_pw_w", "w_out", "w_down")
    bf16_shard = {k: weights[k][0].astype(BF16) for k in big}
    natural = lambda k, g: g.reshape(-1, g.shape[2]) if k in row_sharded else g
    first_srcs = [bf16_shard["w_in"], conv_dw_w[0], ffn_conv_w[0]]
    first_lands = [lax.empty((N_CHIPS,) + s.shape, s.dtype) for s in first_srcs]
    (first_hop,) = _split_start("gather_in_start", [(first_srcs, first_lands, 4 * len(first_srcs), _first_hop_copies)], core_arr)
    launched = first_hop["token"]
    late_sets = dict(mix=["w_attn_out", "conv_pw_w", "w_out"], up=["w_up"], down=["w_down"])
    late_groups = []
    for keys in late_sets.values():
        srcs = [bf16_shard[k] for k in keys]
        late_groups.append((srcs, [lax.empty((N_CHIPS,) + s.shape, BF16) for s in srcs], 4 * len(keys), _gather_copies))
    started = {}

    def late_weights(tag, after):
        if tag == "in":
            w_in_halves, dw4, fc4 = _split_wait("gather_in_wait", first_hop, _first_hop_copies, after)[len(first_srcs):]
            second_hop, *late = _split_start("gather_in_pass_start", [([], [w_in_halves], 3, _second_hop_copies)] + late_groups, dw4)
            started.update(zip(late_sets, late))
            (w_in_full,) = _split_wait("gather_in_pass_wait", second_hop, _second_hop_copies, second_hop["tile"])
            return dict(w_in=w_in_full, conv_dw_w=jnp.concatenate(list(dw4), axis=1), ffn_conv_w=jnp.concatenate(list(fc4), axis=1))
        landed = _split_wait(f"gather_{tag}_wait", started[tag], _gather_copies, after)[len(late_sets[tag]):]
        return {k: natural(k, g) for k, g in zip(late_sets[tag], landed)}

    chip_core = jnp.stack([chip, ci]).astype(jnp.int32)
    exchanging, pending = {}, {}

    held = []

    def launch(tag, after, carry=None):
        keys, groups, partial = [], [], {}
        for k in list(exchanging):
            gk, r1 = _split_wait(f"sibling_exchange_wait_{k}", exchanging.pop(k), _sibling_copies, after)
            partial[k], s16 = _pair_sum_call(gk, r1, core_arr, f"pair_sum_{k}")
            keys.append(k)
            groups.append(([s16], [lax.empty((3,) + s16.shape[1:], BF16)], 3, _exchange_copies))
        fresh = [k for k, _ in held]
        for _, g3 in held:
            groups.append(([g3], [lax.empty((N_CHIPS, g3.shape[1] // 2, g3.shape[2]), F32)], 1, _sibling_copies))
        held.clear()
        begun = _split_start(f"grad_exchange_start_{tag}", groups, core_arr, carry)
        for k, st in zip(keys + fresh, begun):
            if k in partial:
                pending[k] = (partial[k], st)
            else:
                exchanging[k] = st
        return begun[0]["token"] if carry is None else (begun[0]["token"], begun[0]["carry"])

    def on_grad(k, g3, carry=None):
        if k is None:
            return launch("last", g3[:SUBLANES, :LANES])
        held.append((k, g3))
        if k in ("w_out", "w_attn_out"):
            return jnp.float32(0.0)
        return launch(k, g3[0, :SUBLANES, :LANES], carry)

    def finish(keys, after, tag):
        halves = []
        for k in keys:
            s32, st = pending[k]
            recv2 = _split_wait(f"chip_exchange_wait_{k}", st, _exchange_copies, after)[1]
            halves.append(_chip_sum_call(s32, recv2, chip_core, f"chip_sum_{k}"))
        return dict(zip(keys, _sibling_assemble_call(halves, f"grad_sibling_assemble_{tag}")))

    P = dict(b_gate=b_gate, rel_bias=rel_bias, conv_dw_b=conv_dw_b, conv_ln_g=conv_ln_g, conv_ln_b=conv_ln_b,
             norm_mix_pre=norm_mix_pre + launched, norm_mix_post=norm_mix_post, norm_ffn_pre=norm_ffn_pre,
             norm_ffn_post=norm_ffn_post, ffn_conv_b=ffn_conv_b)
    loss_tile, grad_x, G = _local_step(xs, target, P, late_weights, on_grad)

    small = [k for k in names if k not in big]
    packed = _pack([loss_tile[:1]] + [G[k] for k in small])
    (allsum,) = _split_start("small_grad_allsum_start",
                             [([packed], [jnp.zeros((N_DEVICES,) + packed.shape, F32)], N_DEVICES - 1, _allsum_copies)], core_arr)

    reduced, grads, deltas, new_m, new_v = {}, {}, {}, {}, {}

    def update(keys):
        for k in keys:
            gk, d, mn, vn = _adamw_call(weights[k][0], reduced[k], m_in[k][0], v_in[k][0], f"adamw_{k}")
            grads[k], deltas[k], new_m[k], new_v[k] = gk[None], d[None], mn[None], vn[None]

    others = [k for k in big if k != "w_in"]
    reduced.update(finish(others, allsum["tile"], "others"))
    update(others)
    reduced.update(finish(["w_in"], deltas["w_up"], "w_in"))
    update(["w_in"])

    me = jnp.reshape(4 * xi + 2 * yi + ci, (1,)).astype(jnp.int32)
    mine, landed = _split_wait("small_grad_allsum_wait", allsum, _allsum_copies, deltas["w_in"])
    summed_block = _ordered_sum_call(mine, landed, me)
    loss_row, *summed = _unpack(summed_block, [(1, LANES)] + [G[k].shape for k in small])
    loss = loss_row[0, 0]
    for k, gsum in zip(small, summed):
        if k in ("conv_dw_w", "ffn_conv_w"):
            cols = weights[k].shape[2]
            reduced[k] = lax.dynamic_slice_in_dim(gsum, chip * cols, cols, axis=1)
        else:
            reduced[k] = gsum
    for k in small:
        grads[k] = reduced[k].reshape(weights[k].shape)
    ds, mns, vns = _adamw_small_call([weights[k] for k in small], [grads[k] for k in small],
                                     [m_in[k] for k in small], [v_in[k] for k in small])
    deltas.update(zip(small, ds))
    new_m.update(zip(small, mns))
    new_v.update(zip(small, vns))

    return (loss, grad_x[None], *[grads[k] for k in names], *[deltas[k] for k in names],
            *[new_m[k] for k in names], *[new_v[k] for k in names])
```

```python
import functools
import math

import jax
import jax.numpy as jnp
import numpy as np
from jax import lax
from jax.experimental import pallas as pl
from jax.experimental.pallas import tpu as pltpu

F32 = jnp.float32
BF16 = jnp.bfloat16
MESH = pl.DeviceIdType.MESH

HEAD_DIM = 128
HEADS_PER_GROUP = 4
DILATED_PATTERNS = ((128, 1), (512, 4), (2048, 16))
N_GROUPS = 3
N_HEADS = N_GROUPS * HEADS_PER_GROUP
SPAN = 128
GROUP_WIDTH = HEADS_PER_GROUP * HEAD_DIM
CONV_WIDTH = 31
FFN_CONV_WIDTH = 3
N_BUCKETS = 32
MAX_DISTANCE = 2048
RMS_EPS = 1e-6
LN_EPS = 1e-5
NEG_INF = -1e30
ADAM_LR = 0.001
ADAM_B1 = 0.9
ADAM_B2 = 0.999
ADAM_EPS = 1e-08
ADAM_WD = 0.01
ADAM_STEP = 10

LANES = 128
SUBLANES = 8
ROW_TILE = 512
GATE_ROWS, GATE_COLS = 512, 512
TIME_BLOCK = 128
CONV_PAD = 32
FFN_PAD = 8
VMEM_LIMIT = 56 << 20


def _params(sem=None, vmem=None):
    kw = {}
    if sem is not None:
        kw["dimension_semantics"] = sem
    if vmem is not None:
        kw["vmem_limit_bytes"] = vmem
    return pltpu.CompilerParams(**kw)


def _pick(n, cands):
    for c in cands:
        if n % c == 0:
            return c
    return n


ELEMENTWISE_TILE_BYTES = 3 << 19


def _row_tile(rows, cols):
    for align in (16, SUBLANES):
        fits = [t for t in range(align, rows + 1, align) if rows % t == 0 and t * cols * 4 <= ELEMENTWISE_TILE_BYTES]
        if fits:
            return max(fits)
    return SUBLANES


N_CHIPS = 4
M_TILES = (1024, 1408, 512, 256, 128)
N_TILES = (1024, 512, 1408, 256, 128)
K_TILES = (2176, 2048, 1408, 1024, 512, 256, 128)


def _matmul(a, b, mode, name, out_shards=False, tm=None):
    assert a.dtype == BF16 and b.dtype == BF16, (name, a.dtype, b.dtype)
    b3 = b.ndim == 3
    tn = tk = None
    halves = None
    if mode == "nn":
        M, K = a.shape
        N = b.shape[-1] * (N_CHIPS if b3 else 1)
        tn = b.shape[-1] if b3 else None
    elif mode == "nt":
        if a.ndim == 3:
            halves = a.shape[2]
        M, K = a.shape[-2], a.shape[-1] * (a.shape[0] if a.ndim == 3 else 1)
        N = b.shape[-2]
        tk = b.shape[-1] if b3 else None
    else:
        if b3:
            halves = b.shape[2]
        K, M = a.shape
        N = b.shape[-1] * (b.shape[0] if b3 else 1)
        tn = N // N_CHIPS if out_shards else None
    tm = tm or _pick(M, M_TILES)
    tn = tn or _pick(N, N_TILES)
    tk = tk or _pick(K, K_TILES)
    nk = K // tk
    dn = {"nn": (((1,), (0,)), ((), ())), "nt": (((1,), (1,)), ((), ())), "tn": (((0,), (0,)), ((), ()))}[mode]

    def body(a_ref, b_ref, o_ref):
        if nk == 1:
            o_ref[...] = lax.dot_general(a_ref[...], b_ref[...], dn, preferred_element_type=F32)
        else:
            @pl.when(pl.program_id(2) == 0)
            def _():
                o_ref[...] = jnp.zeros_like(o_ref)

            o_ref[...] += lax.dot_general(a_ref[...], b_ref[...], dn, preferred_element_type=F32)

    if mode == "tn":
        a_spec = pl.BlockSpec((tk, tm), lambda i, j, k: (k, i))
    elif halves:
        per = halves // tk
        a_spec = pl.BlockSpec((None, tm, tk), lambda i, j, k: (k // per, i, k % per))
    else:
        a_spec = pl.BlockSpec((tm, tk), lambda i, j, k: (i, k))
    if mode == "nn":
        b_spec = pl.BlockSpec((None, tk, tn), lambda i, j, k: (j, k, 0)) if b3 else pl.BlockSpec((tk, tn), lambda i, j, k: (k, j))
    elif mode == "nt":
        b_spec = pl.BlockSpec((None, tn, tk), lambda i, j, k: (k, j, 0)) if b3 else pl.BlockSpec((tn, tk), lambda i, j, k: (j, k))
    elif halves:
        per = halves // tn
        b_spec = pl.BlockSpec((None, tk, tn), lambda i, j, k: (j // per, k, j % per))
    else:
        b_spec = pl.BlockSpec((tk, tn), lambda i, j, k: (k, j))
    if out_shards:
        out_spec = pl.BlockSpec((None, tm, tn), lambda i, j, k: (j, i, 0))
        out_shape = jax.ShapeDtypeStruct((N_CHIPS, M, tn), F32)
    else:
        out_spec = pl.BlockSpec((tm, tn), lambda i, j, k: (i, j))
        out_shape = jax.ShapeDtypeStruct((M, N), F32)
    return pl.pallas_call(
        body, name=name, grid=(M // tm, N // tn, nk),
        in_specs=[a_spec, b_spec], out_specs=out_spec, out_shape=out_shape,
        compiler_params=_params(("parallel", "parallel", "arbitrary"), VMEM_LIMIT),
    )(a, b)


def _rms(x, g):
    r = lax.rsqrt(jnp.mean(x * x, axis=-1, keepdims=True) + RMS_EPS)
    return x * r * g


def _rms_bwd(x, g, dy):
    r = lax.rsqrt(jnp.mean(x * x, axis=-1, keepdims=True) + RMS_EPS)
    n = x * r
    dn = dy * g
    dx = r * (dn - n * jnp.mean(dn * n, axis=-1, keepdims=True))
    return dx, jnp.sum(dy * n, axis=0, keepdims=True)


def _sigmoid(x):
    return 1.0 / (1.0 + jnp.exp(-x))


_GELU_C = math.sqrt(2.0 / math.pi)


def _gelu(x):
    return 0.5 * x * (1.0 + jnp.tanh(_GELU_C * (x + 0.044715 * x * x * x)))


def _gelu_and_grad(x):
    x2 = x * x
    t = jnp.tanh(_GELU_C * x * (1.0 + 0.044715 * x2))
    half = 0.5 * (1.0 + t)
    return x * half, half + (0.5 * _GELU_C) * x * (1.0 - t * t) * (1.0 + (3.0 * 0.044715) * x2)


def _row_spec(width, col_block=0):
    return pl.BlockSpec((ROW_TILE, width), lambda i: (i, col_block))


def _vec_spec(width, col_block=0):
    return pl.BlockSpec((1, width), lambda i: (0, col_block))


def _accumulate(ref, part):
    @pl.when(pl.program_id(0) == 0)
    def _():
        ref[...] = part

    @pl.when(pl.program_id(0) > 0)
    def _():
        ref[...] += part


def _rms_fwd_call(x, g):
    S, D = x.shape

    def body(x_ref, g_ref, h_ref):
        h_ref[...] = _rms(x_ref[...], g_ref[...]).astype(BF16)

    return pl.pallas_call(
        body, name="rms_mix_pre", grid=(S // ROW_TILE,),
        in_specs=[_row_spec(D), _vec_spec(D)], out_specs=_row_spec(D),
        out_shape=jax.ShapeDtypeStruct((S, D), BF16),
        compiler_params=_params(("parallel",)),
    )(x, g)


def _ln_silu_call(c1, g, b):
    S, C = c1.shape

    def body(c_ref, g_ref, b_ref, o_ref):
        xv = c_ref[...]
        mu = jnp.mean(xv, axis=-1, keepdims=True)
        xc = xv - mu
        var = jnp.mean(xc * xc, axis=-1, keepdims=True)
        z = xc * lax.rsqrt(var + LN_EPS) * g_ref[...] + b_ref[...]
        o_ref[...] = (z * _sigmoid(z)).astype(BF16)

    return pl.pallas_call(
        body, name="conv_ln_silu", grid=(S // ROW_TILE,),
        in_specs=[_row_spec(C), _vec_spec(C), _vec_spec(C)], out_specs=_row_spec(C),
        out_shape=jax.ShapeDtypeStruct((S, C), BF16),
        compiler_params=_params(("parallel",)),
    )(c1, g, b)


def _ln_silu_bwd_call(c1, g, b, dc):
    S, C = c1.shape

    def body(c_ref, g_ref, b_ref, dc_ref, dx_ref, dg_ref, db_ref):
        xv = c_ref[...]
        mu = jnp.mean(xv, axis=-1, keepdims=True)
        xc = xv - mu
        rs = lax.rsqrt(jnp.mean(xc * xc, axis=-1, keepdims=True) + LN_EPS)
        xh = xc * rs
        z = xh * g_ref[...] + b_ref[...]
        sg = _sigmoid(z)
        dz = dc_ref[...] * (sg * (1.0 + z * (1.0 - sg)))
        dxh = dz * g_ref[...]
        dx_ref[...] = rs * (dxh - jnp.mean(dxh, axis=-1, keepdims=True) - xh * jnp.mean(dxh * xh, axis=-1, keepdims=True))
        _accumulate(dg_ref, jnp.sum(dz * xh, axis=0, keepdims=True))
        _accumulate(db_ref, jnp.sum(dz, axis=0, keepdims=True))

    return pl.pallas_call(
        body, name="conv_ln_silu_bwd", grid=(S // ROW_TILE,),
        in_specs=[_row_spec(C), _vec_spec(C), _vec_spec(C), _row_spec(C)],
        out_specs=[_row_spec(C), _vec_spec(C), _vec_spec(C)],
        out_shape=[jax.ShapeDtypeStruct((S, C), F32), jax.ShapeDtypeStruct((1, C), F32), jax.ShapeDtypeStruct((1, C), F32)],
        compiler_params=_params(("arbitrary",)),
    )(c1, g, b, dc)


def _mix_call(proj, gate_col0, b_gate, y_a, y_c):
    S, D = y_a.shape
    w = GATE_COLS
    nc = D // w
    ga0, gc0 = gate_col0 // w, (gate_col0 + D) // w

    def body(ga_ref, gc_ref, ba_ref, bc_ref, ya_ref, yc_ref, o_ref):
        o_ref[...] = (_sigmoid(ga_ref[...] + ba_ref[...]) * ya_ref[...]
                      + _sigmoid(gc_ref[...] + bc_ref[...]) * yc_ref[...]).astype(BF16)

    tile = lambda off: pl.BlockSpec((GATE_ROWS, w), lambda i, j: (i, off + j))
    vec = lambda off: pl.BlockSpec((1, w), lambda i, j: (0, off + j))
    return pl.pallas_call(
        body, name="gate_mix", grid=(S // GATE_ROWS, nc),
        in_specs=[tile(ga0), tile(gc0), vec(0), vec(nc), tile(0), tile(0)],
        out_specs=tile(0), out_shape=jax.ShapeDtypeStruct((S, D), BF16),
        compiler_params=_params(("parallel", "parallel")),
    )(proj, proj, b_gate, b_gate, y_a, y_c)


def _mix_bwd_call(dmixed, proj, gate_col0, b_gate, y_a, y_c):
    S, D = y_a.shape
    w = GATE_COLS
    nc = D // w
    ga0, gc0 = gate_col0 // w, (gate_col0 + D) // w

    def body(dm_ref, ga_ref, gc_ref, ba_ref, bc_ref, ya_ref, yc_ref, dya_ref, dyc_ref, dga_ref, dgc_ref, dba_ref, dbc_ref):
        dm = dm_ref[...]
        sa = _sigmoid(ga_ref[...] + ba_ref[...])
        sc = _sigmoid(gc_ref[...] + bc_ref[...])
        dya_ref[...] = (dm * sa).astype(BF16)
        dyc_ref[...] = (dm * sc).astype(BF16)
        dga = dm * ya_ref[...] * sa * (1.0 - sa)
        dgc = dm * yc_ref[...] * sc * (1.0 - sc)
        dga_ref[...] = dga.astype(BF16)
        dgc_ref[...] = dgc.astype(BF16)
        pa = jnp.sum(dga, axis=0, keepdims=True)
        pc = jnp.sum(dgc, axis=0, keepdims=True)

        @pl.when(pl.program_id(1) == 0)
        def _():
            dba_ref[...] = pa
            dbc_ref[...] = pc

        @pl.when(pl.program_id(1) > 0)
        def _():
            dba_ref[...] += pa
            dbc_ref[...] += pc

    tile = lambda off: pl.BlockSpec((GATE_ROWS, w), lambda j, i: (i, off + j))
    vec = lambda off: pl.BlockSpec((1, w), lambda j, i: (0, off + j))
    return pl.pallas_call(
        body, name="gate_mix_bwd", grid=(nc, S // GATE_ROWS),
        in_specs=[tile(0), tile(ga0), tile(gc0), vec(0), vec(nc), tile(0), tile(0)],
        out_specs=[tile(0), tile(0), tile(0), tile(0), vec(0), vec(0)],
        out_shape=[jax.ShapeDtypeStruct((S, D), BF16)] * 4 + [
                   jax.ShapeDtypeStruct((1, D), F32), jax.ShapeDtypeStruct((1, D), F32)],
        compiler_params=_params(("parallel", "arbitrary")),
    )(dmixed, proj, proj, b_gate, b_gate, y_a, y_c)


def _res1_call(x, out, g_post, g_pre):
    S, D = x.shape

    def body(x_ref, o_ref, gp_ref, gq_ref, x1_ref, h2_ref):
        x1 = x_ref[...] + _rms(o_ref[...], gp_ref[...])
        x1_ref[...] = x1
        h2_ref[...] = _rms(x1, gq_ref[...]).astype(BF16)

    return pl.pallas_call(
        body, name="residual_mix", grid=(S // ROW_TILE,),
        in_specs=[_row_spec(D), _row_spec(D), _vec_spec(D), _vec_spec(D)],
        out_specs=[_row_spec(D), _row_spec(D)],
        out_shape=[jax.ShapeDtypeStruct((S, D), F32), jax.ShapeDtypeStruct((S, D), BF16)],
        compiler_params=_params(("parallel",)),
    )(x, out, g_post, g_pre)


def _loss_call(y, x1, g_post, target):
    S, D = y.shape

    def body(y_ref, x1_ref, g_ref, t_ref, loss_ref, dx_ref, dy_ref, dg_ref):
        yv, gv = y_ref[...], g_ref[...]
        err = x1_ref[...] + _rms(yv, gv) - t_ref[...]
        dx2 = err * (1.0 / D)
        dx_ref[...] = dx2
        dy, dg = _rms_bwd(yv, gv, dx2)
        dy_ref[...] = dy.astype(BF16)
        _accumulate(dg_ref, dg)
        part = 0.5 * jnp.sum(jnp.mean(err * err, axis=-1, keepdims=True), axis=0, keepdims=True)
        _accumulate(loss_ref, jnp.broadcast_to(part, (SUBLANES, LANES)))

    return pl.pallas_call(
        body, name="residual_ffn_loss", grid=(S // ROW_TILE,),
        in_specs=[_row_spec(D), _row_spec(D), _vec_spec(D), _row_spec(D)],
        out_specs=[pl.BlockSpec((SUBLANES, LANES), lambda i: (0, 0)), _row_spec(D), _row_spec(D), _vec_spec(D)],
        out_shape=[jax.ShapeDtypeStruct((SUBLANES, LANES), F32), jax.ShapeDtypeStruct((S, D), F32),
                   jax.ShapeDtypeStruct((S, D), BF16), jax.ShapeDtypeStruct((1, D), F32)],
        compiler_params=_params(("arbitrary",)),
    )(y, x1, g_post, target)


def _mid_bwd_call(x1, g_pre, dh2, dx2, out, g_post):
    S, D = x1.shape

    def body(x1_ref, gq_ref, dh_ref, dx2_ref, o_ref, gp_ref, dx1_ref, do_ref, dgq_ref, dgp_ref):
        d, dgq = _rms_bwd(x1_ref[...], gq_ref[...], dh_ref[...])
        dx1 = dx2_ref[...] + d
        dx1_ref[...] = dx1
        do, dgp = _rms_bwd(o_ref[...], gp_ref[...], dx1)
        do_ref[...] = do.astype(BF16)
        _accumulate(dgq_ref, dgq)
        _accumulate(dgp_ref, dgp)

    return pl.pallas_call(
        body, name="residual_mix_bwd", grid=(S // ROW_TILE,),
        in_specs=[_row_spec(D), _vec_spec(D), _row_spec(D), _row_spec(D), _row_spec(D), _vec_spec(D)],
        out_specs=[_row_spec(D), _row_spec(D), _vec_spec(D), _vec_spec(D)],
        out_shape=[jax.ShapeDtypeStruct((S, D), F32), jax.ShapeDtypeStruct((S, D), BF16)] + [jax.ShapeDtypeStruct((1, D), F32)] * 2,
        compiler_params=_params(("arbitrary",)),
    )(x1, g_pre, dh2, dx2, out, g_post)


def _in_bwd_call(x, g, dh1, dx1):
    S, D = x.shape

    def body(x_ref, g_ref, dh_ref, dx1_ref, gx_ref, dg_ref):
        d, dg = _rms_bwd(x_ref[...], g_ref[...], dh_ref[...])
        gx_ref[...] = dx1_ref[...] + d
        _accumulate(dg_ref, dg)

    return pl.pallas_call(
        body, name="rms_mix_pre_bwd", grid=(S // ROW_TILE,),
        in_specs=[_row_spec(D), _vec_spec(D), _row_spec(D), _row_spec(D)],
        out_specs=[_row_spec(D), _vec_spec(D)],
        out_shape=[jax.ShapeDtypeStruct((S, D), F32), jax.ShapeDtypeStruct((1, D), F32)],
        compiler_params=_params(("arbitrary",)),
    )(x, g, dh1, dx1)


def _bucket_table(dilation):
    qi = np.arange(SPAN)[:, None]
    ki = np.arange(2 * SPAN)[None, :]
    dist = np.maximum(qi + SPAN - ki, 0) * dilation
    max_exact = N_BUCKETS // 2
    d = np.maximum(dist, 1).astype(np.float64)
    large = max_exact + (np.log(d / max_exact) / math.log(MAX_DISTANCE / max_exact) * (N_BUCKETS - max_exact)).astype(np.int32)
    large = np.minimum(large, N_BUCKETS - 1)
    return np.where(dist < max_exact, dist, large).astype(np.int32)


def _bucket_tables():
    return jnp.asarray(np.stack([_bucket_table(r) for _, r in DILATED_PATTERNS]))


def _bias_table_call(rel_bias, buckets):
    def body(rb_ref, bk_ref, o_ref):
        for h in range(N_HEADS):
            bk = bk_ref[h // HEADS_PER_GROUP]

            def step(b, acc):
                return jnp.where(bk == b, rb_ref[b, h], acc)

            o_ref[h] = lax.fori_loop(0, N_BUCKETS, step, jnp.zeros((SPAN, 2 * SPAN), F32))

    return pl.pallas_call(
        body, name="rel_bias_table",
        in_specs=[pl.BlockSpec(memory_space=pltpu.SMEM), pl.BlockSpec(memory_space=pltpu.VMEM)],
        out_specs=pl.BlockSpec(memory_space=pltpu.VMEM),
        out_shape=jax.ShapeDtypeStruct((N_HEADS, SPAN, 2 * SPAN), F32),
    )(rel_bias, buckets)


def _bias_grad_call(dbias, buckets):
    def body(db_ref, bk_ref, o_ref, rows_ref):
        for h in range(N_HEADS):
            bk = bk_ref[h // HEADS_PER_GROUP]
            dv = db_ref[h]

            def step(b, carry):
                rows_ref[h, b] = jnp.sum(jnp.where(bk == b, dv, 0.0), axis=0, keepdims=True)
                return carry

            lax.fori_loop(0, N_BUCKETS, step, 0)
        o_ref[...] = jnp.sum(rows_ref[...], axis=-1, keepdims=True)

    out = pl.pallas_call(
        body, name="rel_bias_grad",
        in_specs=[pl.BlockSpec(memory_space=pltpu.VMEM), pl.BlockSpec(memory_space=pltpu.VMEM)],
        out_specs=pl.BlockSpec(memory_space=pltpu.VMEM),
        out_shape=jax.ShapeDtypeStruct((N_HEADS, N_BUCKETS, 1, 1), F32),
        scratch_shapes=[pltpu.VMEM((N_HEADS, N_BUCKETS, 1, 2 * SPAN), F32)],
    )(dbias, buckets)
    return out.reshape(N_HEADS, N_BUCKETS).T


def _dot_nt(a, b):
    return lax.dot_general(a, b, (((1,), (1,)), ((), ())), preferred_element_type=F32)


def _dot_nn(a, b):
    return lax.dot_general(a, b, (((1,), (0,)), ((), ())), preferred_element_type=F32)


def _dot_tn(a, b):
    return lax.dot_general(a, b, (((0,), (0,)), ((), ())), preferred_element_type=F32)


def _band_masks(n, nb):
    qi = lax.broadcasted_iota(jnp.int32, (SPAN, SPAN), 0)
    ki = lax.broadcasted_iota(jnp.int32, (SPAN, SPAN), 1)
    prev_ok = jnp.logical_and(ki >= qi, n > 0)
    cur_ok = ki <= qi
    next_ok = jnp.logical_and(ki >= qi, n < nb - 1)
    return prev_ok, cur_ok, next_ok


def _wide_band_mask(n):
    qi = lax.broadcasted_iota(jnp.int32, (SPAN, 2 * SPAN), 0)
    ki = lax.broadcasted_iota(jnp.int32, (SPAN, 2 * SPAN), 1)
    prev_ok = jnp.logical_and(jnp.logical_and(ki < SPAN, ki >= qi), n > 0)
    cur_ok = jnp.logical_and(ki >= SPAN, ki - SPAN <= qi)
    return jnp.logical_or(prev_ok, cur_ok)


def _attn_plan(S, group):
    r = DILATED_PATTERNS[group][1]
    hp, per = (HEADS_PER_GROUP, 1) if r == 1 else (2, 4)
    return r, S // (r * SPAN), hp, per


def _residue_rows(rho, r):
    return slice(None) if r == 1 else pl.ds(rho, SPAN, stride=r)


def _for_residues(r, per, fn):
    if r == per:
        for u in range(per):
            fn(u)
        return

    def step(i, carry):
        for u in range(per):
            fn(i * per + u)
        return carry

    lax.fori_loop(0, r // per, step, 0)


def _attn_fwd_call(proj, bias, group):
    S = proj.shape[0]
    r, nb, hp, per = _attn_plan(S, group)
    scale = HEAD_DIM ** -0.5
    kinds = ("q", "kp", "kc", "vp", "vc") if nb > 1 else ("q", "kc", "vc")

    per_kind = _refs_per_kind(r, hp)

    def body(*refs):
        ins = {kind: refs[i * per_kind:(i + 1) * per_kind] for i, kind in enumerate(kinds)}
        b_ref, o_ref, lse_ref = refs[len(kinds) * per_kind:]
        n = pl.program_id(1)
        prev_ok, cur_ok, _ = _band_masks(n, nb)

        band_ok = _wide_band_mask(n) if nb > 1 else cur_ok

        def residue(rho):
            rows = _residue_rows(rho, r)
            for j in range(hp):
                get = lambda kind: _head_rows(ins[kind], j, rows, r).astype(BF16)
                q = get("q")
                if nb > 1:
                    keys, vals, bias_j = jnp.concatenate([get("kp"), get("kc")], axis=0), jnp.concatenate([get("vp"), get("vc")], axis=0), b_ref[j]
                else:
                    keys, vals, bias_j = get("kc"), get("vc"), b_ref[j, :, SPAN:]
                s = jnp.where(band_ok, _dot_nt(q, keys) * scale + bias_j, NEG_INF)
                m = jnp.max(s, axis=-1, keepdims=True)
                p = jnp.exp(s - m)
                den = jnp.sum(p, axis=-1, keepdims=True)
                o_ref[j, rows, :] = _dot_nn(p.astype(BF16), vals) / den
                lse_ref[j, rows, :] = jnp.broadcast_to(m + jnp.log(den), (SPAN, HEAD_DIM))

        _for_residues(r, per, residue)

    in_specs = [_head_spec(r, nb, hp, kind, group, jj) for kind in kinds for jj in range(per_kind)]
    in_specs.append(pl.BlockSpec((hp, SPAN, 2 * SPAN), lambda j, n: (group * (HEADS_PER_GROUP // hp) + j, 0, 0)))
    out = pl.BlockSpec((hp, r * SPAN, HEAD_DIM), lambda j, n: (j, n, 0))
    return pl.pallas_call(
        body, name=f"attn_fwd_g{group}", grid=(HEADS_PER_GROUP // hp, nb),
        in_specs=in_specs, out_specs=[out] * 2,
        out_shape=[jax.ShapeDtypeStruct((HEADS_PER_GROUP, S, HEAD_DIM), F32)] * 2,
        compiler_params=_params(("parallel", "parallel"), VMEM_LIMIT),
    )(*([proj] * (len(in_specs) - 1)), bias)


_PROJ_PART = dict(q=0, qn=0, kp=1, kc=1, vp=2, vc=2)


def _refs_per_kind(r, hp):
    return 1 if r == 1 else hp


def _head_rows(refs, j, rows, r):
    return refs[0][:, j * HEAD_DIM:(j + 1) * HEAD_DIM] if r == 1 else refs[j][rows, :]


def _head_spec(r, nb, hp, kind, group, jj):
    if kind in _PROJ_PART:
        base = (_PROJ_PART[kind] * N_GROUPS + group) * HEADS_PER_GROUP
    else:
        base = 0
    if kind.endswith("p"):
        row = lambda n: jnp.maximum(n - 1, 0)
    elif kind.endswith("n"):
        row = lambda n: jnp.minimum(n + 1, nb - 1)
    else:
        row = lambda n: n
    if r == 1:
        return pl.BlockSpec((SPAN, hp * HEAD_DIM), lambda j, n: (row(n), base // hp + j))
    return pl.BlockSpec((r * SPAN, HEAD_DIM), lambda j, n: (row(n), base + j * hp + jj))


def _attn_merge_call(parts):
    S = parts[0].shape[1]

    def body(o1, s1, o2, s2, o3, s3, a_ref, ab_ref, lse_ref):
        for j in range(HEADS_PER_GROUP):
            sl = slice(j * HEAD_DIM, (j + 1) * HEAD_DIM)
            mx = jnp.maximum(jnp.maximum(s1[j], s2[j]), s3[j])
            w1 = jnp.exp(s1[j] - mx)
            w2 = jnp.exp(s2[j] - mx)
            w3 = jnp.exp(s3[j] - mx)
            den = w1 + w2 + w3
            a = (w1 * o1[j] + w2 * o2[j] + w3 * o3[j]) / den
            a_ref[:, sl] = a
            ab_ref[:, sl] = a.astype(BF16)
            lse_ref[:, sl] = mx + jnp.log(den)

    heads = pl.BlockSpec((HEADS_PER_GROUP, ROW_TILE, HEAD_DIM), lambda i: (0, i, 0))
    return pl.pallas_call(
        body, name="attn_merge", grid=(S // ROW_TILE,),
        in_specs=[heads] * 6, out_specs=[_row_spec(GROUP_WIDTH)] * 3,
        out_shape=[jax.ShapeDtypeStruct((S, GROUP_WIDTH), F32), jax.ShapeDtypeStruct((S, GROUP_WIDTH), BF16),
                   jax.ShapeDtypeStruct((S, GROUP_WIDTH), F32)],
        compiler_params=_params(("parallel",)),
    )(*parts)


def _attn_delta_call(a, da):
    S = a.shape[0]

    def body(a_ref, da_ref, d_ref):
        for j in range(HEADS_PER_GROUP):
            sl = slice(j * HEAD_DIM, (j + 1) * HEAD_DIM)
            d = jnp.sum(a_ref[:, sl] * da_ref[:, sl], axis=-1, keepdims=True)
            d_ref[:, sl] = jnp.broadcast_to(d, (ROW_TILE, HEAD_DIM))

    return pl.pallas_call(
        body, name="attn_delta", grid=(S // ROW_TILE,),
        in_specs=[_row_spec(GROUP_WIDTH)] * 2, out_specs=_row_spec(GROUP_WIDTH),
        out_shape=jax.ShapeDtypeStruct((S, GROUP_WIDTH), F32),
        compiler_params=_params(("parallel",)),
    )(a, da)


def _attn_bwd_call(proj, bias, da, lse, delta, group):
    S = proj.shape[0]
    r, nb, hp, per = _attn_plan(S, group)
    scale = HEAD_DIM ** -0.5
    kinds = ("q", "qn", "kp", "kc", "vp", "vc", "da", "dan", "lse", "lsen", "dl", "dln") if nb > 1 else ("q", "kc", "vc", "da", "lse", "dl")
    source = dict(da=da, dan=da, lse=lse, lsen=lse, dl=delta, dln=delta)

    per_kind = _refs_per_kind(r, hp)

    def body(*refs):
        ins = {kind: refs[i * per_kind:(i + 1) * per_kind] for i, kind in enumerate(kinds)}
        b_ref, dq_ref, dk_ref, dv_ref, db_ref = refs[len(kinds) * per_kind:]
        n = pl.program_id(1)
        prev_ok, cur_ok, next_ok = _band_masks(n, nb)

        @pl.when(n == 0)
        def _():
            db_ref[...] = jnp.zeros_like(db_ref)

        band_ok = _wide_band_mask(n) if nb > 1 else cur_ok

        def residue(rho):
            rows = _residue_rows(rho, r)
            for j in range(hp):
                get = lambda kind: _head_rows(ins[kind], j, rows, r)
                q = get("q").astype(BF16)
                kc = get("kc").astype(BF16)
                vc = get("vc").astype(BF16)
                dav = get("da").astype(BF16)
                lse_q, dl_q = get("lse"), get("dl")
                if nb == 1:
                    pc = jnp.exp(jnp.where(cur_ok, _dot_nt(q, kc) * scale + b_ref[j, :, SPAN:], NEG_INF) - lse_q)
                    dsc = pc * (_dot_nt(dav, vc) - dl_q)
                    dsc_b = dsc.astype(BF16)
                    dq = _dot_nn(dsc_b, kc)
                    dk = _dot_tn(dsc_b, q)
                    dv = _dot_tn(pc.astype(BF16), dav)
                    db_ref[j, :, SPAN:] += dsc
                else:
                    qn = get("qn").astype(BF16)
                    dan = get("dan").astype(BF16)
                    keys = jnp.concatenate([get("kp").astype(BF16), kc], axis=0)
                    vals = jnp.concatenate([get("vp").astype(BF16), vc], axis=0)
                    wide = lambda t: jnp.concatenate([t, t], axis=1)
                    p = jnp.exp(jnp.where(band_ok, _dot_nt(q, keys) * scale + b_ref[j], NEG_INF) - wide(lse_q))
                    ds = p * (_dot_nt(dav, vals) - wide(dl_q))
                    dq = _dot_nn(ds.astype(BF16), keys)
                    db_ref[j] += ds
                    pn = jnp.exp(jnp.where(next_ok, _dot_nt(qn, kc) * scale + b_ref[j, :, :SPAN], NEG_INF) - get("lsen"))
                    dsn = pn * (_dot_nt(dan, vc) - get("dln"))
                    both = lambda cur_part, next_part: jnp.concatenate([cur_part.astype(BF16), next_part.astype(BF16)], axis=0)
                    dk = _dot_tn(both(ds[:, SPAN:], dsn), jnp.concatenate([q, qn], axis=0))
                    dv = _dot_tn(both(p[:, SPAN:], pn), jnp.concatenate([dav, dan], axis=0))
                dq_ref[j, rows, :] = dq * scale
                dk_ref[j, rows, :] = dk * scale
                dv_ref[j, rows, :] = dv

        _for_residues(r, per, residue)

    per_group = HEADS_PER_GROUP // hp
    band = (hp, SPAN, 2 * SPAN)
    in_specs = [_head_spec(r, nb, hp, kind, group, jj) for kind in kinds for jj in range(per_kind)]
    in_specs.append(pl.BlockSpec(band, lambda j, n: (group * per_group + j, 0, 0)))
    operands = [source.get(kind, proj) for kind in kinds for _ in range(per_kind)] + [bias]
    out = pl.BlockSpec((hp, r * SPAN, HEAD_DIM), lambda j, n: (j, n, 0))
    return pl.pallas_call(
        body, name=f"attn_bwd_g{group}", grid=(per_group, nb),
        in_specs=in_specs,
        out_specs=[out] * 3 + [pl.BlockSpec(band, lambda j, n: (j, 0, 0))],
        out_shape=[jax.ShapeDtypeStruct((HEADS_PER_GROUP, S, HEAD_DIM), F32)] * 3
        + [jax.ShapeDtypeStruct((HEADS_PER_GROUP, SPAN, 2 * SPAN), F32)],
        compiler_params=_params(("parallel", "arbitrary"), VMEM_LIMIT),
    )(*operands)


def _dproj_call(dqkv, tails):
    S = tails[0].shape[0]
    width = len(dqkv) * GROUP_WIDTH + sum(t.shape[1] for t in tails)

    def body(*refs):
        o_ref = refs[-1]
        col = 0
        for ref in refs[:len(dqkv)]:
            for j in range(HEADS_PER_GROUP):
                o_ref[:, col:col + HEAD_DIM] = ref[j].astype(BF16)
                col += HEAD_DIM
        for ref in refs[len(dqkv):-1]:
            o_ref[:, col:col + ref.shape[1]] = ref[...]
            col += ref.shape[1]

    heads = pl.BlockSpec((HEADS_PER_GROUP, ROW_TILE, HEAD_DIM), lambda i: (0, i, 0))
    return pl.pallas_call(
        body, name="dproj_assemble", grid=(S // ROW_TILE,),
        in_specs=[heads] * len(dqkv) + [_row_spec(t.shape[1]) for t in tails],
        out_specs=_row_spec(width), out_shape=jax.ShapeDtypeStruct((S, width), BF16),
        compiler_params=_params(("parallel",)),
    )(*dqkv, *tails)


def _tap_rows(xpad_ref, t0, k, width, pad):
    return xpad_ref[pl.ds(t0 + (pad - (width - 1 - k)), TIME_BLOCK), :]


def _conv_block(xpad_ref, t0, w_ref, width, pad):
    acc = None
    for k in range(width):
        term = w_ref[k:k + 1, :] * _tap_rows(xpad_ref, t0, k, width, pad)
        acc = term if acc is None else acc + term
    return acc


def _conv_transpose_block(dpad_ref, t0, w_ref, width):
    acc = None
    for k in range(width):
        term = w_ref[k:k + 1, :] * dpad_ref[pl.ds(t0 + (width - 1 - k), TIME_BLOCK), :]
        acc = term if acc is None else acc + term
    return acc


def _conv_weight_grad(xpad_ref, t0, dy, dw_ref, width, pad):
    for k in range(width):
        dw_ref[k:k + 1, :] += jnp.sum(dy * _tap_rows(xpad_ref, t0, k, width, pad), axis=0, keepdims=True)


def _time_loop(S, step, skip_first=0, skip_last=0):
    def it(tb, carry):
        step(pl.multiple_of(tb * TIME_BLOCK, TIME_BLOCK))
        return carry

    lax.fori_loop(skip_first, S // TIME_BLOCK - skip_last, it, 0)


def _fill_head(head_ref, x_ref, pad):
    head_ref[0:pad, :] = jnp.zeros((pad, LANES), F32)
    head_ref[pad:, :] = x_ref[0:TIME_BLOCK, :]


def _fill_tail(tail_ref, x_ref, pad):
    S = x_ref.shape[0]
    tail_ref[0:TIME_BLOCK, :] = x_ref[S - TIME_BLOCK:S, :]
    tail_ref[TIME_BLOCK:, :] = jnp.zeros((pad, LANES), F32)


def _conv_fwd_call(proj, col0, w, b):
    S = proj.shape[0]
    C = w.shape[1]
    nt = C // LANES
    v0, g0 = col0 // LANES, (col0 + C) // LANES

    def body(val_ref, gate_ref, w_ref, b_ref, o_ref, pad_ref):
        pad_ref[0:CONV_PAD, :] = jnp.zeros((CONV_PAD, LANES), F32)
        pad_ref[CONV_PAD:, :] = val_ref[...] * _sigmoid(gate_ref[...])

        def step(t0):
            o_ref[pl.ds(t0, TIME_BLOCK), :] = _conv_block(pad_ref, t0, w_ref, CONV_WIDTH, CONV_PAD) + b_ref[...]

        _time_loop(S, step)

    seq = lambda off: pl.BlockSpec((S, LANES), lambda i: (0, off + i))
    return pl.pallas_call(
        body, name="conv_module", grid=(nt,),
        in_specs=[seq(v0), seq(g0), pl.BlockSpec((CONV_WIDTH, LANES), lambda i: (0, i)), pl.BlockSpec((1, LANES), lambda i: (0, i))],
        out_specs=seq(0), out_shape=jax.ShapeDtypeStruct((S, C), F32),
        scratch_shapes=[pltpu.VMEM((S + CONV_PAD, LANES), F32)],
        compiler_params=_params(("parallel",)),
    )(proj, proj, w, b)


def _conv_bwd_call(proj, col0, w, dc1):
    S = proj.shape[0]
    C = w.shape[1]
    nt = C // LANES
    v0, g0 = col0 // LANES, (col0 + C) // LANES

    def body(val_ref, gate_ref, w_ref, dy_ref, dval_ref, dgate_ref, dw_ref, db_ref, xpad_ref, tail_ref, dwacc_ref):
        xpad_ref[0:CONV_PAD, :] = jnp.zeros((CONV_PAD, LANES), F32)
        xpad_ref[CONV_PAD:, :] = val_ref[...] * _sigmoid(gate_ref[...])
        _fill_tail(tail_ref, dy_ref, CONV_PAD)
        dwacc_ref[...] = jnp.zeros_like(dwacc_ref)

        def block(t0, dy_src, dy_t0):
            rows = pl.ds(t0, TIME_BLOCK)
            _conv_weight_grad(xpad_ref, t0, dy_ref[rows, :], dwacc_ref, CONV_WIDTH, CONV_PAD)
            dc0 = _conv_transpose_block(dy_src, dy_t0, w_ref, CONV_WIDTH)
            sg = _sigmoid(gate_ref[rows, :])
            dval_ref[rows, :] = (dc0 * sg).astype(BF16)
            dgate_ref[rows, :] = (dc0 * val_ref[rows, :] * sg * (1.0 - sg)).astype(BF16)

        _time_loop(S, lambda t0: block(t0, dy_ref, t0), skip_last=1)
        block(S - TIME_BLOCK, tail_ref, 0)
        dw_ref[...] = dwacc_ref[...]
        db_ref[...] = jnp.sum(dy_ref[...], axis=0, keepdims=True)

    seq = lambda off: pl.BlockSpec((S, LANES), lambda i: (0, off + i))
    return pl.pallas_call(
        body, name="conv_module_bwd", grid=(nt,),
        in_specs=[seq(v0), seq(g0), pl.BlockSpec((CONV_WIDTH, LANES), lambda i: (0, i)), seq(0)],
        out_specs=[seq(0), seq(0), pl.BlockSpec((CONV_PAD, LANES), lambda i: (0, i)), pl.BlockSpec((1, LANES), lambda i: (0, i))],
        out_shape=[jax.ShapeDtypeStruct((S, C), BF16), jax.ShapeDtypeStruct((S, C), BF16),
                   jax.ShapeDtypeStruct((CONV_PAD, C), F32), jax.ShapeDtypeStruct((1, C), F32)],
        scratch_shapes=[pltpu.VMEM((S + CONV_PAD, LANES), F32), pltpu.VMEM((TIME_BLOCK + CONV_PAD, LANES), F32),
                        pltpu.VMEM((CONV_PAD, LANES), F32)],
        compiler_params=_params(("parallel",)),
    )(proj, proj, w, dc1)


def _ffn_fwd_call(u, w, b):
    S, C2 = u.shape
    C = C2 // 2
    nt = C // LANES

    def body(ug_ref, uv_ref, wg_ref, wv_ref, bg_ref, bv_ref, f_ref, hg_ref, hv_ref):
        _fill_head(hg_ref, ug_ref, FFN_PAD)
        _fill_head(hv_ref, uv_ref, FFN_PAD)

        def block(t0, xg_ref, xv_ref, x_t0, pad):
            cg = _conv_block(xg_ref, x_t0, wg_ref, FFN_CONV_WIDTH, pad) + bg_ref[...]
            cv = _conv_block(xv_ref, x_t0, wv_ref, FFN_CONV_WIDTH, pad) + bv_ref[...]
            f_ref[pl.ds(t0, TIME_BLOCK), :] = (_gelu(cg) * cv).astype(BF16)

        block(0, hg_ref, hv_ref, 0, FFN_PAD)
        _time_loop(S, lambda t0: block(t0, ug_ref, uv_ref, t0, 0), skip_first=1)

    seq = lambda off: pl.BlockSpec((S, LANES), lambda i: (0, off + i))
    wsp = lambda off: pl.BlockSpec((FFN_CONV_WIDTH, LANES), lambda i: (0, off + i))
    bsp = lambda off: pl.BlockSpec((1, LANES), lambda i: (0, off + i))
    return pl.pallas_call(
        body, name="ffn_conv_geglu", grid=(nt,),
        in_specs=[seq(0), seq(nt), wsp(0), wsp(nt), bsp(0), bsp(nt)],
        out_specs=seq(0), out_shape=jax.ShapeDtypeStruct((S, C), BF16),
        scratch_shapes=[pltpu.VMEM((FFN_PAD + TIME_BLOCK, LANES), F32)] * 2,
        compiler_params=_params(("parallel",)),
    )(u, u, w, w, b, b)


def _ffn_bwd_call(u, w, b, df):
    S, C2 = u.shape
    C = C2 // 2
    nt = C // LANES

    def body(ug_ref, uv_ref, wg_ref, wv_ref, bg_ref, bv_ref, df_ref,
             du_ref, dwg_ref, dwv_ref, dbg_ref, dbv_ref,
             hg_ref, hv_ref, dg_ref, dv_ref, dwg_acc, dwv_acc, dbg_acc, dbv_acc):
        zeros = jnp.zeros((FFN_PAD, LANES), F32)
        _fill_head(hg_ref, ug_ref, FFN_PAD)
        _fill_head(hv_ref, uv_ref, FFN_PAD)
        dg_ref[S:, :] = zeros
        dv_ref[S:, :] = zeros
        dwg_acc[...] = jnp.zeros_like(dwg_acc)
        dwv_acc[...] = jnp.zeros_like(dwv_acc)
        dbg_acc[...] = jnp.zeros_like(dbg_acc)
        dbv_acc[...] = jnp.zeros_like(dbv_acc)

        def first(t0, xg_ref, xv_ref, x_t0, pad):
            rows = pl.ds(t0, TIME_BLOCK)
            cg = _conv_block(xg_ref, x_t0, wg_ref, FFN_CONV_WIDTH, pad) + bg_ref[...]
            cv = _conv_block(xv_ref, x_t0, wv_ref, FFN_CONV_WIDTH, pad) + bv_ref[...]
            dfb = df_ref[rows, :]
            gelu, gelu_grad = _gelu_and_grad(cg)
            dcg = dfb * cv * gelu_grad
            dcv = dfb * gelu
            dg_ref[rows, :] = dcg
            dv_ref[rows, :] = dcv
            _conv_weight_grad(xg_ref, x_t0, dcg, dwg_acc, FFN_CONV_WIDTH, pad)
            _conv_weight_grad(xv_ref, x_t0, dcv, dwv_acc, FFN_CONV_WIDTH, pad)
            dbg_acc[...] += jnp.sum(dcg, axis=0, keepdims=True)
            dbv_acc[...] += jnp.sum(dcv, axis=0, keepdims=True)

        def second(t0):
            rows = pl.ds(t0, TIME_BLOCK)
            du_ref[0, rows, :] = _conv_transpose_block(dg_ref, t0, wg_ref, FFN_CONV_WIDTH).astype(BF16)
            du_ref[1, rows, :] = _conv_transpose_block(dv_ref, t0, wv_ref, FFN_CONV_WIDTH).astype(BF16)

        first(0, hg_ref, hv_ref, 0, FFN_PAD)
        _time_loop(S, lambda t0: first(t0, ug_ref, uv_ref, t0, 0), skip_first=1)
        _time_loop(S, second)
        dwg_ref[...] = dwg_acc[...]
        dwv_ref[...] = dwv_acc[...]
        dbg_ref[...] = dbg_acc[...]
        dbv_ref[...] = dbv_acc[...]

    seq = lambda off: pl.BlockSpec((S, LANES), lambda i: (0, off + i))
    wsp = lambda off: pl.BlockSpec((FFN_CONV_WIDTH, LANES), lambda i: (0, off + i))
    bsp = lambda off: pl.BlockSpec((1, LANES), lambda i: (0, off + i))
    return pl.pallas_call(
        body, name="ffn_conv_geglu_bwd", grid=(nt,),
        in_specs=[seq(0), seq(nt), wsp(0), wsp(nt), bsp(0), bsp(nt), seq(0)],
        out_specs=[pl.BlockSpec((2, S, LANES), lambda i: (0, 0, i)),
                   pl.BlockSpec((SUBLANES, LANES), lambda i: (0, i)), pl.BlockSpec((SUBLANES, LANES), lambda i: (0, i)),
                   bsp(0), bsp(0)],
        out_shape=[jax.ShapeDtypeStruct((2, S, C), BF16)] + [jax.ShapeDtypeStruct((SUBLANES, C), F32)] * 2
        + [jax.ShapeDtypeStruct((1, C), F32)] * 2,
        scratch_shapes=[pltpu.VMEM((FFN_PAD + TIME_BLOCK, LANES), F32)] * 2 + [pltpu.VMEM((S + FFN_PAD, LANES), F32)] * 2
        + [pltpu.VMEM((SUBLANES, LANES), F32)] * 2
        + [pltpu.VMEM((1, LANES), F32)] * 2,
        compiler_params=_params(("parallel",)),
    )(u, u, w, w, b, b, df)


def _adamw(w_ref, g_ref, m_ref, v_ref, d_ref, mo_ref, vo_ref):
    gv = g_ref[...]
    mn = ADAM_B1 * m_ref[...] + (1.0 - ADAM_B1) * gv
    vn = ADAM_B2 * v_ref[...] + (1.0 - ADAM_B2) * (gv * gv)
    mo_ref[...] = mn
    vo_ref[...] = vn
    m_hat = mn * (1.0 / (1.0 - ADAM_B1 ** ADAM_STEP))
    v_hat = vn * (1.0 / (1.0 - ADAM_B2 ** ADAM_STEP))
    d_ref[...] = -ADAM_LR * (m_hat / (jnp.sqrt(v_hat) + ADAM_EPS) + ADAM_WD * w_ref[...])


def _adamw_call(w, g, m, v, name):
    R, C = w.shape
    tr = _row_tile(R, C)

    def body(w_ref, g_ref, m_ref, v_ref, go_ref, d_ref, mo_ref, vo_ref):
        go_ref[...] = g_ref[...]
        _adamw(w_ref, g_ref, m_ref, v_ref, d_ref, mo_ref, vo_ref)

    spec = pl.BlockSpec((tr, C), lambda i: (i, 0))
    return pl.pallas_call(
        body, name=name, grid=(R // tr,),
        in_specs=[spec] * 4, out_specs=[spec] * 4,
        out_shape=[jax.ShapeDtypeStruct((R, C), F32)] * 4,
        compiler_params=_params(("parallel",)),
    )(w, g, m, v)


def _adamw_small_call(ws, gs, ms, vs):
    n = len(ws)

    def body(*refs):
        w_refs, g_refs, m_refs, v_refs, d_refs, mo_refs, vo_refs = (refs[i * n:(i + 1) * n] for i in range(7))
        for i in range(n):
            _adamw(w_refs[i], g_refs[i], m_refs[i], v_refs[i], d_refs[i], mo_refs[i], vo_refs[i])

    whole = pl.BlockSpec(memory_space=pltpu.VMEM)
    outs = pl.pallas_call(
        body, name="adamw_small",
        in_specs=[whole] * (4 * n), out_specs=[whole] * (3 * n),
        out_shape=[jax.ShapeDtypeStruct(w.shape, F32) for w in ws] * 3,
    )(*ws, *gs, *ms, *vs)
    return outs[:n], outs[n:2 * n], outs[2 * n:]


def _position():
    return lax.axis_index("x"), lax.axis_index("y"), lax.axis_index("c")


def _chip_peers(x, y):
    return [(x, 1 - y), (1 - x, y), (1 - x, 1 - y)]


def _half_rows(ref, core, rows):
    h = rows // 2
    start = pl.multiple_of(core * h, 16)
    return ref.at[pl.ds(start, h), :] if len(ref.shape) == 2 else ref.at[:, pl.ds(start, h), :]


def _shard_half(ref, shard, core, rows):
    h = rows // 2
    return ref.at[shard, pl.ds(pl.multiple_of(core * h, 16), h), :]


ANY = pl.BlockSpec(memory_space=pl.ANY)


def _first_hop_copies(srcs, lands):
    x, y, c = _position()
    chip = 2 * x + y
    targets = [(px, py, c) for px, py in _chip_peers(x, y)] + [(x, y, 1 - c)]
    rows = srcs[0].shape[0]
    out = []
    for i, (s, l) in enumerate(zip(srcs, lands)):
        for k, dev in enumerate(targets):
            if i == 0 and k < 3:
                out.append((_half_rows(s, c, rows), _shard_half(l, chip, c, rows), dev, k))
            else:
                out.append((s, l.at[chip], dev, len(targets) * i + k))
    return out


def _second_hop_copies(srcs, lands):
    x, y, c = _position()
    rows = lands[0].shape[1]
    out = []
    for k, (px, py) in enumerate(_chip_peers(x, y)):
        half = _shard_half(lands[0], 2 * px + py, c, rows)
        out.append((half, half, (x, y, 1 - c), k))
    return out


HBM_SPEC = pl.BlockSpec(memory_space=pltpu.HBM)
SEM_SPEC = pl.BlockSpec(memory_space=pltpu.SEMAPHORE)
DATAFLOW = pltpu.SideEffectType.DATAFLOW_SIDE_EFFECTING


def _in_hbm(a):
    return pltpu.with_memory_space_constraint(a, pltpu.HBM)


def _split_start(name, groups, after, carry=None):
    spans, arrays = [], []
    for srcs, lands, _, _ in groups:
        spans.append((len(arrays), len(srcs), len(lands)))
        arrays += list(srcs) + list(lands)
    if carry is not None:
        arrays.append(carry)
    na, ng = len(arrays), len(groups)

    def body(*refs):
        sems, token = refs[na + 1:na + 1 + 2 * ng], refs[-1]
        for g, (_, _, _, copies) in enumerate(groups):
            off, ns, nl = spans[g]
            for src, dst, dev, idx in copies(refs[off:off + ns], refs[off + ns:off + ns + nl]):
                pltpu.make_async_remote_copy(src_ref=src, dst_ref=dst, send_sem=sems[2 * g].at[idx], recv_sem=sems[2 * g + 1].at[idx],
                                             device_id=dev, device_id_type=MESH).start()
        token[...] = jnp.zeros_like(token)

    outs = pl.pallas_call(
        body, name=name,
        in_specs=[HBM_SPEC] * na + [ANY],
        out_specs=[SEM_SPEC] * (2 * ng) + [HBM_SPEC] * na + [pl.BlockSpec(memory_space=pltpu.VMEM)],
        out_shape=[pltpu.SemaphoreType.DMA((n_sems,)) for _, _, n_sems, _ in groups for _ in range(2)]
        + [pltpu.HBM(a.shape, a.dtype) for a in arrays] + [jax.ShapeDtypeStruct((SUBLANES, LANES), F32)],
        input_output_aliases={i: 2 * ng + i for i in range(na)},
        compiler_params=pltpu.CompilerParams(has_side_effects=DATAFLOW),
    )(*[_in_hbm(a) for a in arrays], after)
    started = []
    for g, (off, ns, nl) in enumerate(spans):
        thru = outs[2 * ng + off:2 * ng + off + ns + nl]
        started.append(dict(send=outs[2 * g], recv=outs[2 * g + 1], srcs=list(thru[:ns]), lands=list(thru[ns:]),
                            tile=outs[-1], token=outs[-1][0, 0], carry=None if carry is None else outs[2 * ng + na - 1]))
    return started


def _split_wait(name, started, copies, after):
    n, m = len(started["srcs"]), len(started["lands"])

    def body(*refs):
        src_refs, land_refs = refs[:n], refs[n:n + m]
        send_sem, recv_sem = refs[n + m], refs[n + m + 1]
        for src, dst, dev, idx in copies(src_refs, land_refs):
            cp = pltpu.make_async_remote_copy(src_ref=src, dst_ref=dst, send_sem=send_sem.at[idx], recv_sem=recv_sem.at[idx],
                                              device_id=dev, device_id_type=MESH)
            cp.wait_send()
            cp.wait_recv()

    arrays = started["srcs"] + started["lands"]
    outs = pl.pallas_call(
        body, name=name,
        in_specs=[HBM_SPEC] * (n + m) + [SEM_SPEC, SEM_SPEC, ANY],
        out_specs=[HBM_SPEC] * (n + m),
        out_shape=[pltpu.HBM(a.shape, a.dtype) for a in arrays],
        input_output_aliases={i: i for i in range(n + m)},
        compiler_params=pltpu.CompilerParams(has_side_effects=DATAFLOW),
    )(*arrays, started["send"], started["recv"], after)
    return list(outs)


def _gather_copies(srcs, lands):
    x, y, c = _position()
    chip = 2 * x + y
    targets = [(px, py, c) for px, py in _chip_peers(x, y)] + [(x, y, 1 - c)]
    return [(s, l.at[chip], dev, len(targets) * i + k) for i, (s, l) in enumerate(zip(srcs, lands)) for k, dev in enumerate(targets)]


def _sibling_copies(srcs, lands):
    x, y, c = _position()
    return [(_half_rows(srcs[0], 1 - c, srcs[0].shape[1]), lands[0], (x, y, 1 - c), 0)]


def _exchange_copies(srcs, lands):
    x, y, c = _position()
    return [(srcs[0].at[2 * px + py], lands[0].at[k], (px, py, c), k) for k, (px, py) in enumerate(_chip_peers(x, y))]


def _pair_sum_call(grad, recv, core, name):
    _, h, B = recv.shape
    tr = _row_tile(h, B)

    def body(core_ref, g_ref, r_ref, o_ref, ob_ref):
        s = g_ref[...] + r_ref[...]
        o_ref[...] = s
        ob_ref[...] = s.astype(BF16)

    g_spec = pl.BlockSpec((None, tr, B), lambda q, i, core_ref: (q, core_ref[0] * (h // tr) + i, 0))
    spec = pl.BlockSpec((None, tr, B), lambda q, i, core_ref: (q, i, 0))
    return pl.pallas_call(
        body, name=name,
        grid_spec=pltpu.PrefetchScalarGridSpec(num_scalar_prefetch=1, grid=(N_CHIPS, h // tr), in_specs=[g_spec, spec],
                                               out_specs=[spec, spec]),
        out_shape=[jax.ShapeDtypeStruct(recv.shape, F32), jax.ShapeDtypeStruct(recv.shape, BF16)],
        compiler_params=_params(("parallel", "parallel")),
    )(core, grad, recv)


def _chip_sum_call(partial, recv, chip_core, name):
    _, h, B = recv.shape
    tr = _row_tile(h, B)

    def body(cc_ref, p_ref, r_ref, o_ref):
        o_ref[...] = ((p_ref[...] + r_ref[0].astype(F32)) + r_ref[1].astype(F32)) + r_ref[2].astype(F32)

    return pl.pallas_call(
        body, name=name,
        grid_spec=pltpu.PrefetchScalarGridSpec(
            num_scalar_prefetch=1, grid=(h // tr,),
            in_specs=[pl.BlockSpec((None, tr, B), lambda i, cc_ref: (cc_ref[0], i, 0)),
                      pl.BlockSpec((3, tr, B), lambda i, cc_ref: (0, i, 0))],
            out_specs=pl.BlockSpec((tr, B), lambda i, cc_ref: (cc_ref[1] * (h // tr) + i, 0))),
        out_shape=jax.ShapeDtypeStruct((2 * h, B), F32),
        compiler_params=_params(("parallel",)),
    )(chip_core, partial, recv)


def _sibling_assemble_call(shards, name="grad_sibling_assemble"):
    n = len(shards)

    def body(*refs):
        ins, outs = refs[:n], refs[n:2 * n]
        send_sems, recv_sems = refs[2 * n:]
        x, y, c = _position()
        copies = []
        for i in range(n):
            rows = shards[i].shape[0]
            cp = pltpu.make_async_remote_copy(src_ref=_half_rows(ins[i], c, rows), dst_ref=_half_rows(outs[i], c, rows),
                                              send_sem=send_sems.at[i], recv_sem=recv_sems.at[i],
                                              device_id=(x, y, 1 - c), device_id_type=MESH)
            cp.start()
            copies.append(cp)
        for cp in copies:
            cp.wait()

    return pl.pallas_call(
        body, name=name,
        in_specs=[ANY] * n, out_specs=[ANY] * n,
        out_shape=[jax.ShapeDtypeStruct(s.shape, F32) for s in shards],
        input_output_aliases={i: i for i in range(n)},
        scratch_shapes=[pltpu.SemaphoreType.DMA((n,)), pltpu.SemaphoreType.DMA((n,))],
    )(*shards)


N_DEVICES = 8


def _allsum_copies(srcs, lands):
    x, y, c = _position()
    me = 4 * x + 2 * y + c
    out = []
    for k in range(1, N_DEVICES):
        peer = (1 - x if k & 4 else x, 1 - y if k & 2 else y, 1 - c if k & 1 else c)
        out.append((srcs[0], lands[0].at[me], peer, k - 1))
    return out


def _ordered_sum_call(mine, landed, me):
    rows = mine.shape[0]

    def body(me_ref, x_ref, l_ref, o_ref):
        acc = jnp.where(me_ref[0] == 0, x_ref[...], l_ref[0])
        for d in range(1, N_DEVICES):
            acc = acc + jnp.where(me_ref[0] == d, x_ref[...], l_ref[d])
        o_ref[...] = acc

    return pl.pallas_call(
        body, name="small_grad_sum",
        in_specs=[pl.BlockSpec(memory_space=pltpu.SMEM), pl.BlockSpec(memory_space=pltpu.VMEM), pl.BlockSpec(memory_space=pltpu.VMEM)],
        out_specs=pl.BlockSpec(memory_space=pltpu.VMEM),
        out_shape=jax.ShapeDtypeStruct((rows, LANES), F32),
    )(me, mine, landed)


def _pack(arrays):
    flat = jnp.concatenate([a.reshape(-1).astype(F32) for a in arrays])
    rows = -(-flat.shape[0] // LANES)
    rows = -(-rows // SUBLANES) * SUBLANES
    flat = jnp.pad(flat, (0, rows * LANES - flat.shape[0]))
    return flat.reshape(rows, LANES)


def _unpack(packed, shapes):
    flat = packed.reshape(-1)
    out, off = [], 0
    for shp in shapes:
        size = int(np.prod(shp))
        out.append(flat[off:off + size].reshape(shp))
        off += size
    return out


def _local_step(xs, target, P, late_weights, on_grad):
    S, D = xs.shape
    qkv_width = 3 * N_HEADS * HEAD_DIM
    glu_col0, gate_col0 = qkv_width, qkv_width + 2 * D
    shard_major = lambda g: g.reshape(N_CHIPS, g.shape[0] // N_CHIPS, g.shape[1])

    h1 = _rms_fwd_call(xs, P["norm_mix_pre"])
    buckets = _bucket_tables()
    bias = _bias_table_call(P["rel_bias"] + 0.0 * h1[0, 0].astype(F32), buckets)
    P = dict(P, **late_weights("in", bias))
    proj = _matmul(h1, P["w_in"], "nn", "proj_in")
    parts = []
    for g in range(N_GROUPS):
        parts += _attn_fwd_call(proj, bias, g)
    a, a_bf, lse = _attn_merge_call(parts)
    P = dict(P, **late_weights("mix", a_bf))
    y_a = _matmul(a_bf, P["w_attn_out"], "nn", "attn_out")
    c1 = _conv_fwd_call(proj, glu_col0, P["conv_dw_w"], P["conv_dw_b"])
    cact = _ln_silu_call(c1, P["conv_ln_g"], P["conv_ln_b"])
    y_c = _matmul(cact, P["conv_pw_w"], "nn", "conv_pw")
    mixed = _mix_call(proj, gate_col0, P["b_gate"], y_a, y_c)
    out = _matmul(mixed, P["w_out"], "nn", "mix_out")
    x1, h2 = _res1_call(xs, out, P["norm_mix_post"], P["norm_ffn_pre"])
    P = dict(P, **late_weights("up", h2))
    u = _matmul(h2, P["w_up"], "nn", "ffn_up")
    f = _ffn_fwd_call(u, P["ffn_conv_w"], P["ffn_conv_b"])
    P = dict(P, **late_weights("down", f))
    yff = _matmul(f, P["w_down"], "nn", "ffn_down")
    loss_tile, dx2, dyff, dg_ffn_post = _loss_call(yff, x1, P["norm_ffn_post"], target)

    G = {}
    G["norm_ffn_post"] = dg_ffn_post
    zero = on_grad("w_down", shard_major(_matmul(f, dyff, "tn", "ffn_down_dw")))
    df = _matmul(dyff, P["w_down"], "nt", "ffn_down_dx")
    du, dwg, dwv, dbg, dbv = _ffn_bwd_call(u, P["ffn_conv_w"], P["ffn_conv_b"] + zero, df)
    G["ffn_conv_w"] = jnp.concatenate([dwg[:FFN_CONV_WIDTH], dwv[:FFN_CONV_WIDTH]], axis=1)
    G["ffn_conv_b"] = jnp.concatenate([dbg, dbv], axis=1)
    zero = on_grad("w_up", _matmul(h2, du, "tn", "ffn_up_dw", out_shards=True))
    dh2 = _matmul(du, P["w_up"], "nt", "ffn_up_dx")
    dx1, dout, G["norm_ffn_pre"], G["norm_mix_post"] = _mid_bwd_call(x1, P["norm_ffn_pre"] + zero, dh2, dx2, out, P["norm_mix_post"])
    zero = on_grad("w_out", shard_major(_matmul(mixed, dout, "tn", "mix_out_dw")))
    dmixed = _matmul(dout, P["w_out"], "nt", "mix_out_dx")
    dya, dyc, dga, dgc, dba, dbc = _mix_bwd_call(dmixed, proj, gate_col0, P["b_gate"] + zero, y_a, y_c)
    G["b_gate"] = jnp.concatenate([dba, dbc], axis=1)
    zero = on_grad("w_attn_out", _matmul(a_bf, dya, "tn", "attn_out_dw", out_shards=True))
    zero = zero + on_grad("conv_pw_w", shard_major(_matmul(cact, dyc, "tn", "conv_pw_dw")))
    da = _matmul(dya, P["w_attn_out"], "nt", "attn_out_dx")
    dcact = _matmul(dyc, P["conv_pw_w"], "nt", "conv_pw_dx")
    dc1, G["conv_ln_g"], G["conv_ln_b"] = _ln_silu_bwd_call(c1, P["conv_ln_g"] + zero, P["conv_ln_b"], dcact)
    dval, dgate, dw_dw, G["conv_dw_b"] = _conv_bwd_call(proj, glu_col0, P["conv_dw_w"], dc1)
    G["conv_dw_w"] = dw_dw[:CONV_WIDTH]
    delta = _attn_delta_call(a, da)
    dqs, dks, dvs, dbs = [], [], [], []
    for g in range(N_GROUPS):
        dq, dk, dv, db = _attn_bwd_call(proj, bias, da, lse, delta, g)
        dqs.append(dq)
        dks.append(dk)
        dvs.append(dv)
        dbs.append(db)
    G["rel_bias"] = _bias_grad_call(jnp.concatenate(dbs, axis=0), buckets)
    dproj = _dproj_call(dqs + dks + dvs, [dval, dgate, dga, dgc])
    zero, dproj = on_grad("w_in", _matmul(h1, dproj, "tn", "proj_in_dw", out_shards=True), carry=dproj)
    dh1 = _matmul(dproj, P["w_in"], "nt", "proj_in_dx")
    zero = zero + on_grad(None, dh1)
    grad_x, G["norm_mix_pre"] = _in_bwd_call(xs, P["norm_mix_pre"] + zero, dh1, dx1)
    return loss_tile, grad_x, G


def kernel(x, w_in, b_gate, rel_bias, w_attn_out, conv_dw_w, conv_dw_b, conv_ln_g, conv_ln_b, conv_pw_w, w_out, norm_mix_pre, norm_mix_post, norm_ffn_pre, norm_ffn_post, w_up, ffn_conv_w, ffn_conv_b, w_down, loss_target, m_w_in, m_b_gate, m_rel_bias, m_w_attn_out, m_conv_dw_w, m_conv_dw_b, m_conv_ln_g, m_conv_ln_b, m_conv_pw_w, m_w_out, m_norm_mix_pre, m_norm_mix_post, m_norm_ffn_pre, m_norm_ffn_post, m_w_up, m_ffn_conv_w, m_ffn_conv_b, m_w_down, v_w_in, v_b_gate, v_rel_bias, v_w_attn_out, v_conv_dw_w, v_conv_dw_b, v_conv_ln_g, v_conv_ln_b, v_conv_pw_w, v_w_out, v_norm_mix_pre, v_norm_mix_post, v_norm_ffn_pre, v_norm_ffn_post, v_w_up, v_ffn_conv_w, v_ffn_conv_b, v_w_down):
    weights = dict(w_in=w_in, b_gate=b_gate, rel_bias=rel_bias, w_attn_out=w_attn_out, conv_dw_w=conv_dw_w, conv_dw_b=conv_dw_b,
                   conv_ln_g=conv_ln_g, conv_ln_b=conv_ln_b, conv_pw_w=conv_pw_w, w_out=w_out, norm_mix_pre=norm_mix_pre,
                   norm_mix_post=norm_mix_post, norm_ffn_pre=norm_ffn_pre, norm_ffn_post=norm_ffn_post, w_up=w_up,
                   ffn_conv_w=ffn_conv_w, ffn_conv_b=ffn_conv_b, w_down=w_down)
    m_in = dict(w_in=m_w_in, b_gate=m_b_gate, rel_bias=m_rel_bias, w_attn_out=m_w_attn_out, conv_dw_w=m_conv_dw_w,
                conv_dw_b=m_conv_dw_b, conv_ln_g=m_conv_ln_g, conv_ln_b=m_conv_ln_b, conv_pw_w=m_conv_pw_w, w_out=m_w_out,
                norm_mix_pre=m_norm_mix_pre, norm_mix_post=m_norm_mix_post, norm_ffn_pre=m_norm_ffn_pre,
                norm_ffn_post=m_norm_ffn_post, w_up=m_w_up, ffn_conv_w=m_ffn_conv_w, ffn_conv_b=m_ffn_conv_b, w_down=m_w_down)
    v_in = dict(w_in=v_w_in, b_gate=v_b_gate, rel_bias=v_rel_bias, w_attn_out=v_w_attn_out, conv_dw_w=v_conv_dw_w,
                conv_dw_b=v_conv_dw_b, conv_ln_g=v_conv_ln_g, conv_ln_b=v_conv_ln_b, conv_pw_w=v_conv_pw_w, w_out=v_w_out,
                norm_mix_pre=v_norm_mix_pre, norm_mix_post=v_norm_mix_post, norm_ffn_pre=v_norm_ffn_pre,
                norm_ffn_post=v_norm_ffn_post, w_up=v_w_up, ffn_conv_w=v_ffn_conv_w, ffn_conv_b=v_ffn_conv_b, w_down=v_w_down)
    names = list(weights)
    xi, yi, ci = _position()
    chip = 2 * xi + yi
    core_arr = jnp.reshape(ci, (1,)).astype(jnp.int32)

    xs = x[0]
    target = loss_target[0]
    S, D = xs.shape

    big = ["w_in", "w_attn_out", "conv_pw_w", "w_out", "w_up", "w_down"]
    row_sharded = ("conv_pw_w", "w_out", "w_down")
    bf16_shard = {k: weights[k][0].astype(BF16) for k in big}
    natural = lambda k, g: g.reshape(-1, g.shape[2]) if k in row_sharded else g
    first_srcs = [bf16_shard["w_in"], conv_dw_w[0], ffn_conv_w[0]]
    first_lands = [lax.empty((N_CHIPS,) + s.shape, s.dtype) for s in first_srcs]
    (first_hop,) = _split_start("gather_in_start", [(first_srcs, first_lands, 4 * len(first_srcs), _first_hop_copies)], core_arr)
    launched = first_hop["token"]
    late_sets = dict(mix=["w_attn_out", "conv_pw_w", "w_out"], up=["w_up"], down=["w_down"])
    late_groups = []
    for keys in late_sets.values():
        srcs = [bf16_shard[k] for k in keys]
        late_groups.append((srcs, [lax.empty((N_CHIPS,) + s.shape, BF16) for s in srcs], 4 * len(keys), _gather_copies))
    started = {}

    def late_weights(tag, after):
        if tag == "in":
            w_in_halves, dw4, fc4 = _split_wait("gather_in_wait", first_hop, _first_hop_copies, after)[len(first_srcs):]
            second_hop, *late = _split_start("gather_in_pass_start", [([], [w_in_halves], 3, _second_hop_copies)] + late_groups, dw4)
            started.update(zip(late_sets, late))
            (w_in_full,) = _split_wait("gather_in_pass_wait", second_hop, _second_hop_copies, second_hop["tile"])
            return dict(w_in=w_in_full, conv_dw_w=jnp.concatenate(list(dw4), axis=1), ffn_conv_w=jnp.concatenate(list(fc4), axis=1))
        landed = _split_wait(f"gather_{tag}_wait", started[tag], _gather_copies, after)[len(late_sets[tag]):]
        return {k: natural(k, g) for k, g in zip(late_sets[tag], landed)}

    chip_core = jnp.stack([chip, ci]).astype(jnp.int32)
    exchanging, pending = {}, {}

    held = []

    def launch(tag, after, carry=None):
        keys, groups, partial = [], [], {}
        for k in list(exchanging):
            gk, r1 = _split_wait(f"sibling_exchange_wait_{k}", exchanging.pop(k), _sibling_copies, after)
            partial[k], s16 = _pair_sum_call(gk, r1, core_arr, f"pair_sum_{k}")
            keys.append(k)
            groups.append(([s16], [lax.empty((3,) + s16.shape[1:], BF16)], 3, _exchange_copies))
        fresh = [k for k, _ in held]
        for _, g3 in held:
            groups.append(([g3], [lax.empty((N_CHIPS, g3.shape[1] // 2, g3.shape[2]), F32)], 1, _sibling_copies))
        held.clear()
        begun = _split_start(f"grad_exchange_start_{tag}", groups, core_arr, carry)
        for k, st in zip(keys + fresh, begun):
            if k in partial:
                pending[k] = (partial[k], st)
            else:
                exchanging[k] = st
        return begun[0]["token"] if carry is None else (begun[0]["token"], begun[0]["carry"])

    def on_grad(k, g3, carry=None):
        if k is None:
            return launch("last", g3[:SUBLANES, :LANES])
        held.append((k, g3))
        if k in ("w_down", "w_out", "w_attn_out"):
            return jnp.float32(0.0)
        return launch(k, g3[0, :SUBLANES, :LANES], carry)

    def finish(keys, after, tag):
        halves = []
        for k in keys:
            s32, st = pending[k]
            recv2 = _split_wait(f"chip_exchange_wait_{k}", st, _exchange_copies, after)[1]
            halves.append(_chip_sum_call(s32, recv2, chip_core, f"chip_sum_{k}"))
        return dict(zip(keys, _sibling_assemble_call(halves, f"grad_sibling_assemble_{tag}")))

    P = dict(b_gate=b_gate, rel_bias=rel_bias, conv_dw_b=conv_dw_b, conv_ln_g=conv_ln_g, conv_ln_b=conv_ln_b,
             norm_mix_pre=norm_mix_pre + launched, norm_mix_post=norm_mix_post, norm_ffn_pre=norm_ffn_pre,
             norm_ffn_post=norm_ffn_post, ffn_conv_b=ffn_conv_b)
    loss_tile, grad_x, G = _local_step(xs, target, P, late_weights, on_grad)

    small = [k for k in names if k not in big]
    packed = _pack([loss_tile[:1]] + [G[k] for k in small])
    (allsum,) = _split_start("small_grad_allsum_start",
                             [([packed], [jnp.zeros((N_DEVICES,) + packed.shape, F32)], N_DEVICES - 1, _allsum_copies)], core_arr)

    reduced, grads, deltas, new_m, new_v = {}, {}, {}, {}, {}

    def update(keys):
        for k in keys:
            gk, d, mn, vn = _adamw_call(weights[k][0], reduced[k], m_in[k][0], v_in[k][0], f"adamw_{k}")
            grads[k], deltas[k], new_m[k], new_v[k] = gk[None], d[None], mn[None], vn[None]

    others = [k for k in big if k != "w_in"]
    reduced.update(finish(others, allsum["tile"], "others"))
    update(others)
    reduced.update(finish(["w_in"], deltas["w_up"], "w_in"))
    update(["w_in"])

    me = jnp.reshape(4 * xi + 2 * yi + ci, (1,)).astype(jnp.int32)
    mine, landed = _split_wait("small_grad_allsum_wait", allsum, _allsum_copies, deltas["w_in"])
    summed_block = _ordered_sum_call(mine, landed, me)
    loss_row, *summed = _unpack(summed_block, [(1, LANES)] + [G[k].shape for k in small])
    loss = loss_row[0, 0]
    for k, gsum in zip(small, summed):
        if k in ("conv_dw_w", "ffn_conv_w"):
            cols = weights[k].shape[2]
            reduced[k] = lax.dynamic_slice_in_dim(gsum, chip * cols, cols, axis=1)
        else:
            reduced[k] = gsum
    for k in small:
        grads[k] = reduced[k].reshape(weights[k].shape)
    ds, mns, vns = _adamw_small_call([weights[k] for k in small], [grads[k] for k in small],
                                     [m_in[k] for k in small], [v_in[k] for k in small])
    deltas.update(zip(small, ds))
    new_m.update(zip(small, mns))
    new_v.update(zip(small, vns))

    return (loss, grad_x[None], *[grads[k] for k in names], *[deltas[k] for k in names],
            *[new_m[k] for k in names], *[new_v[k] for k in names])
```

```python
import functools
import math

import jax
import jax.numpy as jnp
import numpy as np
from jax import lax
from jax.experimental import pallas as pl
from jax.experimental.pallas import tpu as pltpu

F32 = jnp.float32
BF16 = jnp.bfloat16
MESH = pl.DeviceIdType.MESH

HEAD_DIM = 128
HEADS_PER_GROUP = 4
DILATED_PATTERNS = ((128, 1), (512, 4), (2048, 16))
N_GROUPS = 3
N_HEADS = N_GROUPS * HEADS_PER_GROUP
SPAN = 128
GROUP_WIDTH = HEADS_PER_GROUP * HEAD_DIM
CONV_WIDTH = 31
FFN_CONV_WIDTH = 3
N_BUCKETS = 32
MAX_DISTANCE = 2048
RMS_EPS = 1e-6
LN_EPS = 1e-5
NEG_INF = -1e30
ADAM_LR = 0.001
ADAM_B1 = 0.9
ADAM_B2 = 0.999
ADAM_EPS = 1e-08
ADAM_WD = 0.01
ADAM_STEP = 10

LANES = 128
SUBLANES = 8
ROW_TILE = 512
GATE_ROWS, GATE_COLS = 512, 512
TIME_BLOCK = 128
CONV_PAD = 32
FFN_PAD = 8
VMEM_LIMIT = 56 << 20


def _params(sem=None, vmem=None):
    kw = {}
    if sem is not None:
        kw["dimension_semantics"] = sem
    if vmem is not None:
        kw["vmem_limit_bytes"] = vmem
    return pltpu.CompilerParams(**kw)


def _pick(n, cands):
    for c in cands:
        if n % c == 0:
            return c
    return n


ELEMENTWISE_TILE_BYTES = 3 << 19


def _row_tile(rows, cols):
    for align in (16, SUBLANES):
        fits = [t for t in range(align, rows + 1, align) if rows % t == 0 and t * cols * 4 <= ELEMENTWISE_TILE_BYTES]
        if fits:
            return max(fits)
    return SUBLANES


N_CHIPS = 4
M_TILES = (1024, 1408, 512, 256, 128)
N_TILES = (1024, 512, 1408, 256, 128)
K_TILES = (2176, 2048, 1408, 1024, 512, 256, 128)


def _matmul(a, b, mode, name, out_shards=False, tm=None):
    assert a.dtype == BF16 and b.dtype == BF16, (name, a.dtype, b.dtype)
    b3 = b.ndim == 3
    tn = tk = None
    halves = None
    if mode == "nn":
        M, K = a.shape
        N = b.shape[-1] * (N_CHIPS if b3 else 1)
        tn = b.shape[-1] if b3 else None
    elif mode == "nt":
        if a.ndim == 3:
            halves = a.shape[2]
        M, K = a.shape[-2], a.shape[-1] * (a.shape[0] if a.ndim == 3 else 1)
        N = b.shape[-2]
        tk = b.shape[-1] if b3 else None
    else:
        if b3:
            halves = b.shape[2]
        K, M = a.shape
        N = b.shape[-1] * (b.shape[0] if b3 else 1)
        tn = N // N_CHIPS if out_shards else None
    tm = tm or _pick(M, M_TILES)
    tn = tn or _pick(N, N_TILES)
    tk = tk or _pick(K, K_TILES)
    nk = K // tk
    dn = {"nn": (((1,), (0,)), ((), ())), "nt": (((1,), (1,)), ((), ())), "tn": (((0,), (0,)), ((), ()))}[mode]

    def body(a_ref, b_ref, o_ref):
        if nk == 1:
            o_ref[...] = lax.dot_general(a_ref[...], b_ref[...], dn, preferred_element_type=F32)
        else:
            @pl.when(pl.program_id(2) == 0)
            def _():
                o_ref[...] = jnp.zeros_like(o_ref)

            o_ref[...] += lax.dot_general(a_ref[...], b_ref[...], dn, preferred_element_type=F32)

    if mode == "tn":
        a_spec = pl.BlockSpec((tk, tm), lambda i, j, k: (k, i))
    elif halves:
        per = halves // tk
        a_spec = pl.BlockSpec((None, tm, tk), lambda i, j, k: (k // per, i, k % per))
    else:
        a_spec = pl.BlockSpec((tm, tk), lambda i, j, k: (i, k))
    if mode == "nn":
        b_spec = pl.BlockSpec((None, tk, tn), lambda i, j, k: (j, k, 0)) if b3 else pl.BlockSpec((tk, tn), lambda i, j, k: (k, j))
    elif mode == "nt":
        b_spec = pl.BlockSpec((None, tn, tk), lambda i, j, k: (k, j, 0)) if b3 else pl.BlockSpec((tn, tk), lambda i, j, k: (j, k))
    elif halves:
        per = halves // tn
        b_spec = pl.BlockSpec((None, tk, tn), lambda i, j, k: (j // per, k, j % per))
    else:
        b_spec = pl.BlockSpec((tk, tn), lambda i, j, k: (k, j))
    if out_shards:
        out_spec = pl.BlockSpec((None, tm, tn), lambda i, j, k: (j, i, 0))
        out_shape = jax.ShapeDtypeStruct((N_CHIPS, M, tn), F32)
    else:
        out_spec = pl.BlockSpec((tm, tn), lambda i, j, k: (i, j))
        out_shape = jax.ShapeDtypeStruct((M, N), F32)
    return pl.pallas_call(
        body, name=name, grid=(M // tm, N // tn, nk),
        in_specs=[a_spec, b_spec], out_specs=out_spec, out_shape=out_shape,
        compiler_params=_params(("parallel", "parallel", "arbitrary"), VMEM_LIMIT),
    )(a, b)


def _rms(x, g):
    r = lax.rsqrt(jnp.mean(x * x, axis=-1, keepdims=True) + RMS_EPS)
    return x * r * g


def _rms_bwd(x, g, dy):
    r = lax.rsqrt(jnp.mean(x * x, axis=-1, keepdims=True) + RMS_EPS)
    n = x * r
    dn = dy * g
    dx = r * (dn - n * jnp.mean(dn * n, axis=-1, keepdims=True))
    return dx, jnp.sum(dy * n, axis=0, keepdims=True)


def _sigmoid(x):
    return 1.0 / (1.0 + jnp.exp(-x))


_GELU_C = math.sqrt(2.0 / math.pi)


def _gelu(x):
    return 0.5 * x * (1.0 + jnp.tanh(_GELU_C * (x + 0.044715 * x * x * x)))


def _gelu_and_grad(x):
    x2 = x * x
    t = jnp.tanh(_GELU_C * x * (1.0 + 0.044715 * x2))
    half = 0.5 * (1.0 + t)
    return x * half, half + (0.5 * _GELU_C) * x * (1.0 - t * t) * (1.0 + (3.0 * 0.044715) * x2)


def _row_spec(width, col_block=0):
    return pl.BlockSpec((ROW_TILE, width), lambda i: (i, col_block))


def _vec_spec(width, col_block=0):
    return pl.BlockSpec((1, width), lambda i: (0, col_block))


def _accumulate(ref, part):
    @pl.when(pl.program_id(0) == 0)
    def _():
        ref[...] = part

    @pl.when(pl.program_id(0) > 0)
    def _():
        ref[...] += part


def _rms_fwd_call(x, g):
    S, D = x.shape

    def body(x_ref, g_ref, h_ref):
        h_ref[...] = _rms(x_ref[...], g_ref[...]).astype(BF16)

    return pl.pallas_call(
        body, name="rms_mix_pre", grid=(S // ROW_TILE,),
        in_specs=[_row_spec(D), _vec_spec(D)], out_specs=_row_spec(D),
        out_shape=jax.ShapeDtypeStruct((S, D), BF16),
        compiler_params=_params(("parallel",)),
    )(x, g)


def _ln_silu_call(c1, g, b):
    S, C = c1.shape

    def body(c_ref, g_ref, b_ref, o_ref):
        xv = c_ref[...]
        mu = jnp.mean(xv, axis=-1, keepdims=True)
        xc = xv - mu
        var = jnp.mean(xc * xc, axis=-1, keepdims=True)
        z = xc * lax.rsqrt(var + LN_EPS) * g_ref[...] + b_ref[...]
        o_ref[...] = (z * _sigmoid(z)).astype(BF16)

    return pl.pallas_call(
        body, name="conv_ln_silu", grid=(S // ROW_TILE,),
        in_specs=[_row_spec(C), _vec_spec(C), _vec_spec(C)], out_specs=_row_spec(C),
        out_shape=jax.ShapeDtypeStruct((S, C), BF16),
        compiler_params=_params(("parallel",)),
    )(c1, g, b)


def _ln_silu_bwd_call(c1, g, b, dc):
    S, C = c1.shape

    def body(c_ref, g_ref, b_ref, dc_ref, dx_ref, dg_ref, db_ref):
        xv = c_ref[...]
        mu = jnp.mean(xv, axis=-1, keepdims=True)
        xc = xv - mu
        rs = lax.rsqrt(jnp.mean(xc * xc, axis=-1, keepdims=True) + LN_EPS)
        xh = xc * rs
        z = xh * g_ref[...] + b_ref[...]
        sg = _sigmoid(z)
        dz = dc_ref[...] * (sg * (1.0 + z * (1.0 - sg)))
        dxh = dz * g_ref[...]
        dx_ref[...] = rs * (dxh - jnp.mean(dxh, axis=-1, keepdims=True) - xh * jnp.mean(dxh * xh, axis=-1, keepdims=True))
        _accumulate(dg_ref, jnp.sum(dz * xh, axis=0, keepdims=True))
        _accumulate(db_ref, jnp.sum(dz, axis=0, keepdims=True))

    return pl.pallas_call(
        body, name="conv_ln_silu_bwd", grid=(S // ROW_TILE,),
        in_specs=[_row_spec(C), _vec_spec(C), _vec_spec(C), _row_spec(C)],
        out_specs=[_row_spec(C), _vec_spec(C), _vec_spec(C)],
        out_shape=[jax.ShapeDtypeStruct((S, C), F32), jax.ShapeDtypeStruct((1, C), F32), jax.ShapeDtypeStruct((1, C), F32)],
        compiler_params=_params(("arbitrary",)),
    )(c1, g, b, dc)


def _mix_call(proj, gate_col0, b_gate, y_a, y_c):
    S, D = y_a.shape
    w = GATE_COLS
    nc = D // w
    ga0, gc0 = gate_col0 // w, (gate_col0 + D) // w

    def body(ga_ref, gc_ref, ba_ref, bc_ref, ya_ref, yc_ref, o_ref):
        o_ref[...] = (_sigmoid(ga_ref[...] + ba_ref[...]) * ya_ref[...]
                      + _sigmoid(gc_ref[...] + bc_ref[...]) * yc_ref[...]).astype(BF16)

    tile = lambda off: pl.BlockSpec((GATE_ROWS, w), lambda i, j: (i, off + j))
    vec = lambda off: pl.BlockSpec((1, w), lambda i, j: (0, off + j))
    return pl.pallas_call(
        body, name="gate_mix", grid=(S // GATE_ROWS, nc),
        in_specs=[tile(ga0), tile(gc0), vec(0), vec(nc), tile(0), tile(0)],
        out_specs=tile(0), out_shape=jax.ShapeDtypeStruct((S, D), BF16),
        compiler_params=_params(("parallel", "parallel")),
    )(proj, proj, b_gate, b_gate, y_a, y_c)


def _mix_bwd_call(dmixed, proj, gate_col0, b_gate, y_a, y_c):
    S, D = y_a.shape
    w = GATE_COLS
    nc = D // w
    ga0, gc0 = gate_col0 // w, (gate_col0 + D) // w

    def body(dm_ref, ga_ref, gc_ref, ba_ref, bc_ref, ya_ref, yc_ref, dya_ref, dyc_ref, dga_ref, dgc_ref, dba_ref, dbc_ref):
        dm = dm_ref[...]
        sa = _sigmoid(ga_ref[...] + ba_ref[...])
        sc = _sigmoid(gc_ref[...] + bc_ref[...])
        dya_ref[...] = (dm * sa).astype(BF16)
        dyc_ref[...] = (dm * sc).astype(BF16)
        dga = dm * ya_ref[...] * sa * (1.0 - sa)
        dgc = dm * yc_ref[...] * sc * (1.0 - sc)
        dga_ref[...] = dga.astype(BF16)
        dgc_ref[...] = dgc.astype(BF16)
        pa = jnp.sum(dga, axis=0, keepdims=True)
        pc = jnp.sum(dgc, axis=0, keepdims=True)

        @pl.when(pl.program_id(1) == 0)
        def _():
            dba_ref[...] = pa
            dbc_ref[...] = pc

        @pl.when(pl.program_id(1) > 0)
        def _():
            dba_ref[...] += pa
            dbc_ref[...] += pc

    tile = lambda off: pl.BlockSpec((GATE_ROWS, w), lambda j, i: (i, off + j))
    vec = lambda off: pl.BlockSpec((1, w), lambda j, i: (0, off + j))
    return pl.pallas_call(
        body, name="gate_mix_bwd", grid=(nc, S // GATE_ROWS),
        in_specs=[tile(0), tile(ga0), tile(gc0), vec(0), vec(nc), tile(0), tile(0)],
        out_specs=[tile(0), tile(0), tile(0), tile(0), vec(0), vec(0)],
        out_shape=[jax.ShapeDtypeStruct((S, D), BF16)] * 4 + [
                   jax.ShapeDtypeStruct((1, D), F32), jax.ShapeDtypeStruct((1, D), F32)],
        compiler_params=_params(("parallel", "arbitrary")),
    )(dmixed, proj, proj, b_gate, b_gate, y_a, y_c)


def _res1_call(x, out, g_post, g_pre):
    S, D = x.shape

    def body(x_ref, o_ref, gp_ref, gq_ref, x1_ref, h2_ref):
        x1 = x_ref[...] + _rms(o_ref[...], gp_ref[...])
        x1_ref[...] = x1
        h2_ref[...] = _rms(x1, gq_ref[...]).astype(BF16)

    return pl.pallas_call(
        body, name="residual_mix", grid=(S // ROW_TILE,),
        in_specs=[_row_spec(D), _row_spec(D), _vec_spec(D), _vec_spec(D)],
        out_specs=[_row_spec(D), _row_spec(D)],
        out_shape=[jax.ShapeDtypeStruct((S, D), F32), jax.ShapeDtypeStruct((S, D), BF16)],
        compiler_params=_params(("parallel",)),
    )(x, out, g_post, g_pre)


def _loss_call(y, x1, g_post, target):
    S, D = y.shape

    def body(y_ref, x1_ref, g_ref, t_ref, loss_ref, dx_ref, dy_ref, dg_ref):
        yv, gv = y_ref[...], g_ref[...]
        err = x1_ref[...] + _rms(yv, gv) - t_ref[...]
        dx2 = err * (1.0 / D)
        dx_ref[...] = dx2
        dy, dg = _rms_bwd(yv, gv, dx2)
        dy_ref[...] = dy.astype(BF16)
        _accumulate(dg_ref, dg)
        part = 0.5 * jnp.sum(jnp.mean(err * err, axis=-1, keepdims=True), axis=0, keepdims=True)
        _accumulate(loss_ref, jnp.broadcast_to(part, (SUBLANES, LANES)))

    return pl.pallas_call(
        body, name="residual_ffn_loss", grid=(S // ROW_TILE,),
        in_specs=[_row_spec(D), _row_spec(D), _vec_spec(D), _row_spec(D)],
        out_specs=[pl.BlockSpec((SUBLANES, LANES), lambda i: (0, 0)), _row_spec(D), _row_spec(D), _vec_spec(D)],
        out_shape=[jax.ShapeDtypeStruct((SUBLANES, LANES), F32), jax.ShapeDtypeStruct((S, D), F32),
                   jax.ShapeDtypeStruct((S, D), BF16), jax.ShapeDtypeStruct((1, D), F32)],
        compiler_params=_params(("arbitrary",)),
    )(y, x1, g_post, target)


def _mid_bwd_call(x1, g_pre, dh2, dx2, out, g_post):
    S, D = x1.shape

    def body(x1_ref, gq_ref, dh_ref, dx2_ref, o_ref, gp_ref, dx1_ref, do_ref, dgq_ref, dgp_ref):
        d, dgq = _rms_bwd(x1_ref[...], gq_ref[...], dh_ref[...])
        dx1 = dx2_ref[...] + d
        dx1_ref[...] = dx1
        do, dgp = _rms_bwd(o_ref[...], gp_ref[...], dx1)
        do_ref[...] = do.astype(BF16)
        _accumulate(dgq_ref, dgq)
        _accumulate(dgp_ref, dgp)

    return pl.pallas_call(
        body, name="residual_mix_bwd", grid=(S // ROW_TILE,),
        in_specs=[_row_spec(D), _vec_spec(D), _row_spec(D), _row_spec(D), _row_spec(D), _vec_spec(D)],
        out_specs=[_row_spec(D), _row_spec(D), _vec_spec(D), _vec_spec(D)],
        out_shape=[jax.ShapeDtypeStruct((S, D), F32), jax.ShapeDtypeStruct((S, D), BF16)] + [jax.ShapeDtypeStruct((1, D), F32)] * 2,
        compiler_params=_params(("arbitrary",)),
    )(x1, g_pre, dh2, dx2, out, g_post)


def _in_bwd_call(x, g, dh1, dx1):
    S, D = x.shape

    def body(x_ref, g_ref, dh_ref, dx1_ref, gx_ref, dg_ref):
        d, dg = _rms_bwd(x_ref[...], g_ref[...], dh_ref[...])
        gx_ref[...] = dx1_ref[...] + d
        _accumulate(dg_ref, dg)

    return pl.pallas_call(
        body, name="rms_mix_pre_bwd", grid=(S // ROW_TILE,),
        in_specs=[_row_spec(D), _vec_spec(D), _row_spec(D), _row_spec(D)],
        out_specs=[_row_spec(D), _vec_spec(D)],
        out_shape=[jax.ShapeDtypeStruct((S, D), F32), jax.ShapeDtypeStruct((1, D), F32)],
        compiler_params=_params(("arbitrary",)),
    )(x, g, dh1, dx1)


def _bucket_table(dilation):
    qi = np.arange(SPAN)[:, None]
    ki = np.arange(2 * SPAN)[None, :]
    dist = np.maximum(qi + SPAN - ki, 0) * dilation
    max_exact = N_BUCKETS // 2
    d = np.maximum(dist, 1).astype(np.float64)
    large = max_exact + (np.log(d / max_exact) / math.log(MAX_DISTANCE / max_exact) * (N_BUCKETS - max_exact)).astype(np.int32)
    large = np.minimum(large, N_BUCKETS - 1)
    return np.where(dist < max_exact, dist, large).astype(np.int32)


def _bucket_tables():
    return jnp.asarray(np.stack([_bucket_table(r) for _, r in DILATED_PATTERNS]))


def _bias_table_call(rel_bias, buckets):
    def body(rb_ref, bk_ref, o_ref):
        for h in range(N_HEADS):
            bk = bk_ref[h // HEADS_PER_GROUP]

            def step(b, acc):
                return jnp.where(bk == b, rb_ref[b, h], acc)

            o_ref[h] = lax.fori_loop(0, N_BUCKETS, step, jnp.zeros((SPAN, 2 * SPAN), F32))

    return pl.pallas_call(
        body, name="rel_bias_table",
        in_specs=[pl.BlockSpec(memory_space=pltpu.SMEM), pl.BlockSpec(memory_space=pltpu.VMEM)],
        out_specs=pl.BlockSpec(memory_space=pltpu.VMEM),
        out_shape=jax.ShapeDtypeStruct((N_HEADS, SPAN, 2 * SPAN), F32),
    )(rel_bias, buckets)


def _bias_grad_call(dbias, buckets):
    def body(db_ref, bk_ref, o_ref, rows_ref):
        for h in range(N_HEADS):
            bk = bk_ref[h // HEADS_PER_GROUP]
            dv = db_ref[h]

            def step(b, carry):
                rows_ref[h, b] = jnp.sum(jnp.where(bk == b, dv, 0.0), axis=0, keepdims=True)
                return carry

            lax.fori_loop(0, N_BUCKETS, step, 0)
        o_ref[...] = jnp.sum(rows_ref[...], axis=-1, keepdims=True)

    out = pl.pallas_call(
        body, name="rel_bias_grad",
        in_specs=[pl.BlockSpec(memory_space=pltpu.VMEM), pl.BlockSpec(memory_space=pltpu.VMEM)],
        out_specs=pl.BlockSpec(memory_space=pltpu.VMEM),
        out_shape=jax.ShapeDtypeStruct((N_HEADS, N_BUCKETS, 1, 1), F32),
        scratch_shapes=[pltpu.VMEM((N_HEADS, N_BUCKETS, 1, 2 * SPAN), F32)],
    )(dbias, buckets)
    return out.reshape(N_HEADS, N_BUCKETS).T


def _dot_nt(a, b):
    return lax.dot_general(a, b, (((1,), (1,)), ((), ())), preferred_element_type=F32)


def _dot_nn(a, b):
    return lax.dot_general(a, b, (((1,), (0,)), ((), ())), preferred_element_type=F32)


def _dot_tn(a, b):
    return lax.dot_general(a, b, (((0,), (0,)), ((), ())), preferred_element_type=F32)


def _band_masks(n, nb):
    qi = lax.broadcasted_iota(jnp.int32, (SPAN, SPAN), 0)
    ki = lax.broadcasted_iota(jnp.int32, (SPAN, SPAN), 1)
    prev_ok = jnp.logical_and(ki >= qi, n > 0)
    cur_ok = ki <= qi
    next_ok = jnp.logical_and(ki >= qi, n < nb - 1)
    return prev_ok, cur_ok, next_ok


def _wide_band_mask(n):
    qi = lax.broadcasted_iota(jnp.int32, (SPAN, 2 * SPAN), 0)
    ki = lax.broadcasted_iota(jnp.int32, (SPAN, 2 * SPAN), 1)
    prev_ok = jnp.logical_and(jnp.logical_and(ki < SPAN, ki >= qi), n > 0)
    cur_ok = jnp.logical_and(ki >= SPAN, ki - SPAN <= qi)
    return jnp.logical_or(prev_ok, cur_ok)


def _attn_plan(S, group):
    r = DILATED_PATTERNS[group][1]
    hp, per = (HEADS_PER_GROUP, 1) if r == 1 else (2, 4)
    return r, S // (r * SPAN), hp, per


def _residue_rows(rho, r):
    return slice(None) if r == 1 else pl.ds(rho, SPAN, stride=r)


def _for_residues(r, per, fn):
    if r == per:
        for u in range(per):
            fn(u)
        return

    def step(i, carry):
        for u in range(per):
            fn(i * per + u)
        return carry

    lax.fori_loop(0, r // per, step, 0)


def _attn_fwd_call(proj, bias, group):
    S = proj.shape[0]
    r, nb, hp, per = _attn_plan(S, group)
    scale = HEAD_DIM ** -0.5
    kinds = ("q", "kp", "kc", "vp", "vc") if nb > 1 else ("q", "kc", "vc")

    per_kind = _refs_per_kind(r, hp)

    def body(*refs):
        ins = {kind: refs[i * per_kind:(i + 1) * per_kind] for i, kind in enumerate(kinds)}
        b_ref, o_ref, lse_ref = refs[len(kinds) * per_kind:]
        n = pl.program_id(1)
        prev_ok, cur_ok, _ = _band_masks(n, nb)

        band_ok = _wide_band_mask(n) if nb > 1 else cur_ok

        def residue(rho):
            rows = _residue_rows(rho, r)
            for j in range(hp):
                get = lambda kind: _head_rows(ins[kind], j, rows, r).astype(BF16)
                q = get("q")
                if nb > 1:
                    keys, vals, bias_j = jnp.concatenate([get("kp"), get("kc")], axis=0), jnp.concatenate([get("vp"), get("vc")], axis=0), b_ref[j]
                else:
                    keys, vals, bias_j = get("kc"), get("vc"), b_ref[j, :, SPAN:]
                s = jnp.where(band_ok, _dot_nt(q, keys) * scale + bias_j, NEG_INF)
                m = jnp.max(s, axis=-1, keepdims=True)
                p = jnp.exp(s - m)
                den = jnp.sum(p, axis=-1, keepdims=True)
                o_ref[j, rows, :] = _dot_nn(p.astype(BF16), vals) / den
                lse_ref[j, rows, :] = jnp.broadcast_to(m + jnp.log(den), (SPAN, HEAD_DIM))

        _for_residues(r, per, residue)

    in_specs = [_head_spec(r, nb, hp, kind, group, jj) for kind in kinds for jj in range(per_kind)]
    in_specs.append(pl.BlockSpec((hp, SPAN, 2 * SPAN), lambda j, n: (group * (HEADS_PER_GROUP // hp) + j, 0, 0)))
    out = pl.BlockSpec((hp, r * SPAN, HEAD_DIM), lambda j, n: (j, n, 0))
    return pl.pallas_call(
        body, name=f"attn_fwd_g{group}", grid=(HEADS_PER_GROUP // hp, nb),
        in_specs=in_specs, out_specs=[out] * 2,
        out_shape=[jax.ShapeDtypeStruct((HEADS_PER_GROUP, S, HEAD_DIM), F32)] * 2,
        compiler_params=_params(("parallel", "parallel"), VMEM_LIMIT),
    )(*([proj] * (len(in_specs) - 1)), bias)


_PROJ_PART = dict(q=0, qn=0, kp=1, kc=1, vp=2, vc=2)


def _refs_per_kind(r, hp):
    return 1 if r == 1 else hp


def _head_rows(refs, j, rows, r):
    return refs[0][:, j * HEAD_DIM:(j + 1) * HEAD_DIM] if r == 1 else refs[j][rows, :]


def _head_spec(r, nb, hp, kind, group, jj):
    if kind in _PROJ_PART:
        base = (_PROJ_PART[kind] * N_GROUPS + group) * HEADS_PER_GROUP
    else:
        base = 0
    if kind.endswith("p"):
        row = lambda n: jnp.maximum(n - 1, 0)
    elif kind.endswith("n"):
        row = lambda n: jnp.minimum(n + 1, nb - 1)
    else:
        row = lambda n: n
    if r == 1:
        return pl.BlockSpec((SPAN, hp * HEAD_DIM), lambda j, n: (row(n), base // hp + j))
    return pl.BlockSpec((r * SPAN, HEAD_DIM), lambda j, n: (row(n), base + j * hp + jj))


def _attn_merge_call(parts):
    S = parts[0].shape[1]

    def body(o1, s1, o2, s2, o3, s3, a_ref, ab_ref, lse_ref):
        for j in range(HEADS_PER_GROUP):
            sl = slice(j * HEAD_DIM, (j + 1) * HEAD_DIM)
            mx = jnp.maximum(jnp.maximum(s1[j], s2[j]), s3[j])
            w1 = jnp.exp(s1[j] - mx)
            w2 = jnp.exp(s2[j] - mx)
            w3 = jnp.exp(s3[j] - mx)
            den = w1 + w2 + w3
            a = (w1 * o1[j] + w2 * o2[j] + w3 * o3[j]) / den
            a_ref[:, sl] = a
            ab_ref[:, sl] = a.astype(BF16)
            lse_ref[:, sl] = mx + jnp.log(den)

    heads = pl.BlockSpec((HEADS_PER_GROUP, ROW_TILE, HEAD_DIM), lambda i: (0, i, 0))
    return pl.pallas_call(
        body, name="attn_merge", grid=(S // ROW_TILE,),
        in_specs=[heads] * 6, out_specs=[_row_spec(GROUP_WIDTH)] * 3,
        out_shape=[jax.ShapeDtypeStruct((S, GROUP_WIDTH), F32), jax.ShapeDtypeStruct((S, GROUP_WIDTH), BF16),
                   jax.ShapeDtypeStruct((S, GROUP_WIDTH), F32)],
        compiler_params=_params(("parallel",)),
    )(*parts)


def _attn_delta_call(a, da):
    S = a.shape[0]

    def body(a_ref, da_ref, d_ref):
        for j in range(HEADS_PER_GROUP):
            sl = slice(j * HEAD_DIM, (j + 1) * HEAD_DIM)
            d = jnp.sum(a_ref[:, sl] * da_ref[:, sl], axis=-1, keepdims=True)
            d_ref[:, sl] = jnp.broadcast_to(d, (ROW_TILE, HEAD_DIM))

    return pl.pallas_call(
        body, name="attn_delta", grid=(S // ROW_TILE,),
        in_specs=[_row_spec(GROUP_WIDTH)] * 2, out_specs=_row_spec(GROUP_WIDTH),
        out_shape=jax.ShapeDtypeStruct((S, GROUP_WIDTH), F32),
        compiler_params=_params(("parallel",)),
    )(a, da)


def _attn_bwd_call(proj, bias, da, lse, delta, group):
    S = proj.shape[0]
    r, nb, hp, per = _attn_plan(S, group)
    scale = HEAD_DIM ** -0.5
    kinds = ("q", "qn", "kp", "kc", "vp", "vc", "da", "dan", "lse", "lsen", "dl", "dln") if nb > 1 else ("q", "kc", "vc", "da", "lse", "dl")
    source = dict(da=da, dan=da, lse=lse, lsen=lse, dl=delta, dln=delta)

    per_kind = _refs_per_kind(r, hp)

    def body(*refs):
        ins = {kind: refs[i * per_kind:(i + 1) * per_kind] for i, kind in enumerate(kinds)}
        b_ref, dq_ref, dk_ref, dv_ref, db_ref = refs[len(kinds) * per_kind:]
        n = pl.program_id(1)
        prev_ok, cur_ok, next_ok = _band_masks(n, nb)

        @pl.when(n == 0)
        def _():
            db_ref[...] = jnp.zeros_like(db_ref)

        band_ok = _wide_band_mask(n) if nb > 1 else cur_ok

        def residue(rho):
            rows = _residue_rows(rho, r)
            for j in range(hp):
                get = lambda kind: _head_rows(ins[kind], j, rows, r)
                q = get("q").astype(BF16)
                kc = get("kc").astype(BF16)
                vc = get("vc").astype(BF16)
                dav = get("da").astype(BF16)
                lse_q, dl_q = get("lse"), get("dl")
                if nb == 1:
                    pc = jnp.exp(jnp.where(cur_ok, _dot_nt(q, kc) * scale + b_ref[j, :, SPAN:], NEG_INF) - lse_q)
                    dsc = pc * (_dot_nt(dav, vc) - dl_q)
                    dsc_b = dsc.astype(BF16)
                    dq = _dot_nn(dsc_b, kc)
                    dk = _dot_tn(dsc_b, q)
                    dv = _dot_tn(pc.astype(BF16), dav)
                    db_ref[j, :, SPAN:] += dsc
                else:
                    qn = get("qn").astype(BF16)
                    dan = get("dan").astype(BF16)
                    keys = jnp.concatenate([get("kp").astype(BF16), kc], axis=0)
                    vals = jnp.concatenate([get("vp").astype(BF16), vc], axis=0)
                    wide = lambda t: jnp.concatenate([t, t], axis=1)
                    p = jnp.exp(jnp.where(band_ok, _dot_nt(q, keys) * scale + b_ref[j], NEG_INF) - wide(lse_q))
                    ds = p * (_dot_nt(dav, vals) - wide(dl_q))
                    dq = _dot_nn(ds.astype(BF16), keys)
                    db_ref[j] += ds
                    pn = jnp.exp(jnp.where(next_ok, _dot_nt(qn, kc) * scale + b_ref[j, :, :SPAN], NEG_INF) - get("lsen"))
                    dsn = pn * (_dot_nt(dan, vc) - get("dln"))
                    both = lambda cur_part, next_part: jnp.concatenate([cur_part.astype(BF16), next_part.astype(BF16)], axis=0)
                    dk = _dot_tn(both(ds[:, SPAN:], dsn), jnp.concatenate([q, qn], axis=0))
                    dv = _dot_tn(both(p[:, SPAN:], pn), jnp.concatenate([dav, dan], axis=0))
                dq_ref[j, rows, :] = dq * scale
                dk_ref[j, rows, :] = dk * scale
                dv_ref[j, rows, :] = dv

        _for_residues(r, per, residue)

    per_group = HEADS_PER_GROUP // hp
    band = (hp, SPAN, 2 * SPAN)
    in_specs = [_head_spec(r, nb, hp, kind, group, jj) for kind in kinds for jj in range(per_kind)]
    in_specs.append(pl.BlockSpec(band, lambda j, n: (group * per_group + j, 0, 0)))
    operands = [source.get(kind, proj) for kind in kinds for _ in range(per_kind)] + [bias]
    out = pl.BlockSpec((hp, r * SPAN, HEAD_DIM), lambda j, n: (j, n, 0))
    return pl.pallas_call(
        body, name=f"attn_bwd_g{group}", grid=(per_group, nb),
        in_specs=in_specs,
        out_specs=[out] * 3 + [pl.BlockSpec(band, lambda j, n: (j, 0, 0))],
        out_shape=[jax.ShapeDtypeStruct((HEADS_PER_GROUP, S, HEAD_DIM), F32)] * 3
        + [jax.ShapeDtypeStruct((HEADS_PER_GROUP, SPAN, 2 * SPAN), F32)],
        compiler_params=_params(("parallel", "arbitrary"), VMEM_LIMIT),
    )(*operands)


def _dproj_call(dqkv, tails):
    S = tails[0].shape[0]
    width = len(dqkv) * GROUP_WIDTH + sum(t.shape[1] for t in tails)

    def body(*refs):
        o_ref = refs[-1]
        col = 0
        for ref in refs[:len(dqkv)]:
            for j in range(HEADS_PER_GROUP):
                o_ref[:, col:col + HEAD_DIM] = ref[j].astype(BF16)
                col += HEAD_DIM
        for ref in refs[len(dqkv):-1]:
            o_ref[:, col:col + ref.shape[1]] = ref[...]
            col += ref.shape[1]

    heads = pl.BlockSpec((HEADS_PER_GROUP, ROW_TILE, HEAD_DIM), lambda i: (0, i, 0))
    return pl.pallas_call(
        body, name="dproj_assemble", grid=(S // ROW_TILE,),
        in_specs=[heads] * len(dqkv) + [_row_spec(t.shape[1]) for t in tails],
        out_specs=_row_spec(width), out_shape=jax.ShapeDtypeStruct((S, width), BF16),
        compiler_params=_params(("parallel",)),
    )(*dqkv, *tails)


def _tap_rows(xpad_ref, t0, k, width, pad):
    return xpad_ref[pl.ds(t0 + (pad - (width - 1 - k)), TIME_BLOCK), :]


def _conv_block(xpad_ref, t0, w_ref, width, pad):
    acc = None
    for k in range(width):
        term = w_ref[k:k + 1, :] * _tap_rows(xpad_ref, t0, k, width, pad)
        acc = term if acc is None else acc + term
    return acc


def _conv_transpose_block(dpad_ref, t0, w_ref, width):
    acc = None
    for k in range(width):
        term = w_ref[k:k + 1, :] * dpad_ref[pl.ds(t0 + (width - 1 - k), TIME_BLOCK), :]
        acc = term if acc is None else acc + term
    return acc


def _conv_weight_grad(xpad_ref, t0, dy, dw_ref, width, pad):
    for k in range(width):
        dw_ref[k:k + 1, :] += jnp.sum(dy * _tap_rows(xpad_ref, t0, k, width, pad), axis=0, keepdims=True)


def _time_loop(S, step, skip_first=0, skip_last=0):
    def it(tb, carry):
        step(pl.multiple_of(tb * TIME_BLOCK, TIME_BLOCK))
        return carry

    lax.fori_loop(skip_first, S // TIME_BLOCK - skip_last, it, 0)


def _fill_head(head_ref, x_ref, pad):
    head_ref[0:pad, :] = jnp.zeros((pad, LANES), F32)
    head_ref[pad:, :] = x_ref[0:TIME_BLOCK, :]


def _fill_tail(tail_ref, x_ref, pad):
    S = x_ref.shape[0]
    tail_ref[0:TIME_BLOCK, :] = x_ref[S - TIME_BLOCK:S, :]
    tail_ref[TIME_BLOCK:, :] = jnp.zeros((pad, LANES), F32)


def _conv_fwd_call(proj, col0, w, b):
    S = proj.shape[0]
    C = w.shape[1]
    nt = C // LANES
    v0, g0 = col0 // LANES, (col0 + C) // LANES

    def body(val_ref, gate_ref, w_ref, b_ref, o_ref, pad_ref):
        pad_ref[0:CONV_PAD, :] = jnp.zeros((CONV_PAD, LANES), F32)
        pad_ref[CONV_PAD:, :] = val_ref[...] * _sigmoid(gate_ref[...])

        def step(t0):
            o_ref[pl.ds(t0, TIME_BLOCK), :] = _conv_block(pad_ref, t0, w_ref, CONV_WIDTH, CONV_PAD) + b_ref[...]

        _time_loop(S, step)

    seq = lambda off: pl.BlockSpec((S, LANES), lambda i: (0, off + i))
    return pl.pallas_call(
        body, name="conv_module", grid=(nt,),
        in_specs=[seq(v0), seq(g0), pl.BlockSpec((CONV_WIDTH, LANES), lambda i: (0, i)), pl.BlockSpec((1, LANES), lambda i: (0, i))],
        out_specs=seq(0), out_shape=jax.ShapeDtypeStruct((S, C), F32),
        scratch_shapes=[pltpu.VMEM((S + CONV_PAD, LANES), F32)],
        compiler_params=_params(("parallel",)),
    )(proj, proj, w, b)


def _conv_bwd_call(proj, col0, w, dc1):
    S = proj.shape[0]
    C = w.shape[1]
    nt = C // LANES
    v0, g0 = col0 // LANES, (col0 + C) // LANES

    def body(val_ref, gate_ref, w_ref, dy_ref, dval_ref, dgate_ref, dw_ref, db_ref, xpad_ref, tail_ref, dwacc_ref):
        xpad_ref[0:CONV_PAD, :] = jnp.zeros((CONV_PAD, LANES), F32)
        xpad_ref[CONV_PAD:, :] = val_ref[...] * _sigmoid(gate_ref[...])
        _fill_tail(tail_ref, dy_ref, CONV_PAD)
        dwacc_ref[...] = jnp.zeros_like(dwacc_ref)

        def block(t0, dy_src, dy_t0):
            rows = pl.ds(t0, TIME_BLOCK)
            _conv_weight_grad(xpad_ref, t0, dy_ref[rows, :], dwacc_ref, CONV_WIDTH, CONV_PAD)
            dc0 = _conv_transpose_block(dy_src, dy_t0, w_ref, CONV_WIDTH)
            sg = _sigmoid(gate_ref[rows, :])
            dval_ref[rows, :] = (dc0 * sg).astype(BF16)
            dgate_ref[rows, :] = (dc0 * val_ref[rows, :] * sg * (1.0 - sg)).astype(BF16)

        _time_loop(S, lambda t0: block(t0, dy_ref, t0), skip_last=1)
        block(S - TIME_BLOCK, tail_ref, 0)
        dw_ref[...] = dwacc_ref[...]
        db_ref[...] = jnp.sum(dy_ref[...], axis=0, keepdims=True)

    seq = lambda off: pl.BlockSpec((S, LANES), lambda i: (0, off + i))
    return pl.pallas_call(
        body, name="conv_module_bwd", grid=(nt,),
        in_specs=[seq(v0), seq(g0), pl.BlockSpec((CONV_WIDTH, LANES), lambda i: (0, i)), seq(0)],
        out_specs=[seq(0), seq(0), pl.BlockSpec((CONV_PAD, LANES), lambda i: (0, i)), pl.BlockSpec((1, LANES), lambda i: (0, i))],
        out_shape=[jax.ShapeDtypeStruct((S, C), BF16), jax.ShapeDtypeStruct((S, C), BF16),
                   jax.ShapeDtypeStruct((CONV_PAD, C), F32), jax.ShapeDtypeStruct((1, C), F32)],
        scratch_shapes=[pltpu.VMEM((S + CONV_PAD, LANES), F32), pltpu.VMEM((TIME_BLOCK + CONV_PAD, LANES), F32),
                        pltpu.VMEM((CONV_PAD, LANES), F32)],
        compiler_params=_params(("parallel",)),
    )(proj, proj, w, dc1)


def _ffn_fwd_call(u, w, b):
    S, C2 = u.shape
    C = C2 // 2
    nt = C // LANES

    def body(ug_ref, uv_ref, wg_ref, wv_ref, bg_ref, bv_ref, f_ref, hg_ref, hv_ref):
        _fill_head(hg_ref, ug_ref, FFN_PAD)
        _fill_head(hv_ref, uv_ref, FFN_PAD)

        def block(t0, xg_ref, xv_ref, x_t0, pad):
            cg = _conv_block(xg_ref, x_t0, wg_ref, FFN_CONV_WIDTH, pad) + bg_ref[...]
            cv = _conv_block(xv_ref, x_t0, wv_ref, FFN_CONV_WIDTH, pad) + bv_ref[...]
            f_ref[pl.ds(t0, TIME_BLOCK), :] = (_gelu(cg) * cv).astype(BF16)

        block(0, hg_ref, hv_ref, 0, FFN_PAD)
        _time_loop(S, lambda t0: block(t0, ug_ref, uv_ref, t0, 0), skip_first=1)

    seq = lambda off: pl.BlockSpec((S, LANES), lambda i: (0, off + i))
    wsp = lambda off: pl.BlockSpec((FFN_CONV_WIDTH, LANES), lambda i: (0, off + i))
    bsp = lambda off: pl.BlockSpec((1, LANES), lambda i: (0, off + i))
    return pl.pallas_call(
        body, name="ffn_conv_geglu", grid=(nt,),
        in_specs=[seq(0), seq(nt), wsp(0), wsp(nt), bsp(0), bsp(nt)],
        out_specs=seq(0), out_shape=jax.ShapeDtypeStruct((S, C), BF16),
        scratch_shapes=[pltpu.VMEM((FFN_PAD + TIME_BLOCK, LANES), F32)] * 2,
        compiler_params=_params(("parallel",)),
    )(u, u, w, w, b, b)


def _ffn_bwd_call(u, w, b, df):
    S, C2 = u.shape
    C = C2 // 2
    nt = C // LANES

    def body(ug_ref, uv_ref, wg_ref, wv_ref, bg_ref, bv_ref, df_ref,
             du_ref, dwg_ref, dwv_ref, dbg_ref, dbv_ref,
             hg_ref, hv_ref, dg_ref, dv_ref, dwg_acc, dwv_acc, dbg_acc, dbv_acc):
        zeros = jnp.zeros((FFN_PAD, LANES), F32)
        _fill_head(hg_ref, ug_ref, FFN_PAD)
        _fill_head(hv_ref, uv_ref, FFN_PAD)
        dg_ref[S:, :] = zeros
        dv_ref[S:, :] = zeros
        dwg_acc[...] = jnp.zeros_like(dwg_acc)
        dwv_acc[...] = jnp.zeros_like(dwv_acc)
        dbg_acc[...] = jnp.zeros_like(dbg_acc)
        dbv_acc[...] = jnp.zeros_like(dbv_acc)

        def first(t0, xg_ref, xv_ref, x_t0, pad):
            rows = pl.ds(t0, TIME_BLOCK)
            cg = _conv_block(xg_ref, x_t0, wg_ref, FFN_CONV_WIDTH, pad) + bg_ref[...]
            cv = _conv_block(xv_ref, x_t0, wv_ref, FFN_CONV_WIDTH, pad) + bv_ref[...]
            dfb = df_ref[rows, :]
            gelu, gelu_grad = _gelu_and_grad(cg)
            dcg = dfb * cv * gelu_grad
            dcv = dfb * gelu
            dg_ref[rows, :] = dcg
            dv_ref[rows, :] = dcv
            _conv_weight_grad(xg_ref, x_t0, dcg, dwg_acc, FFN_CONV_WIDTH, pad)
            _conv_weight_grad(xv_ref, x_t0, dcv, dwv_acc, FFN_CONV_WIDTH, pad)
            dbg_acc[...] += jnp.sum(dcg, axis=0, keepdims=True)
            dbv_acc[...] += jnp.sum(dcv, axis=0, keepdims=True)

        def second(t0):
            rows = pl.ds(t0, TIME_BLOCK)
            du_ref[0, rows, :] = _conv_transpose_block(dg_ref, t0, wg_ref, FFN_CONV_WIDTH).astype(BF16)
            du_ref[1, rows, :] = _conv_transpose_block(dv_ref, t0, wv_ref, FFN_CONV_WIDTH).astype(BF16)

        first(0, hg_ref, hv_ref, 0, FFN_PAD)
        _time_loop(S, lambda t0: first(t0, ug_ref, uv_ref, t0, 0), skip_first=1)
        _time_loop(S, second)
        dwg_ref[...] = dwg_acc[...]
        dwv_ref[...] = dwv_acc[...]
        dbg_ref[...] = dbg_acc[...]
        dbv_ref[...] = dbv_acc[...]

    seq = lambda off: pl.BlockSpec((S, LANES), lambda i: (0, off + i))
    wsp = lambda off: pl.BlockSpec((FFN_CONV_WIDTH, LANES), lambda i: (0, off + i))
    bsp = lambda off: pl.BlockSpec((1, LANES), lambda i: (0, off + i))
    return pl.pallas_call(
        body, name="ffn_conv_geglu_bwd", grid=(nt,),
        in_specs=[seq(0), seq(nt), wsp(0), wsp(nt), bsp(0), bsp(nt), seq(0)],
        out_specs=[pl.BlockSpec((2, S, LANES), lambda i: (0, 0, i)),
                   pl.BlockSpec((SUBLANES, LANES), lambda i: (0, i)), pl.BlockSpec((SUBLANES, LANES), lambda i: (0, i)),
                   bsp(0), bsp(0)],
        out_shape=[jax.ShapeDtypeStruct((2, S, C), BF16)] + [jax.ShapeDtypeStruct((SUBLANES, C), F32)] * 2
        + [jax.ShapeDtypeStruct((1, C), F32)] * 2,
        scratch_shapes=[pltpu.VMEM((FFN_PAD + TIME_BLOCK, LANES), F32)] * 2 + [pltpu.VMEM((S + FFN_PAD, LANES), F32)] * 2
        + [pltpu.VMEM((SUBLANES, LANES), F32)] * 2
        + [pltpu.VMEM((1, LANES), F32)] * 2,
        compiler_params=_params(("parallel",)),
    )(u, u, w, w, b, b, df)


def _adamw(w_ref, g_ref, m_ref, v_ref, d_ref, mo_ref, vo_ref):
    gv = g_ref[...]
    mn = ADAM_B1 * m_ref[...] + (1.0 - ADAM_B1) * gv
    vn = ADAM_B2 * v_ref[...] + (1.0 - ADAM_B2) * (gv * gv)
    mo_ref[...] = mn
    vo_ref[...] = vn
    m_hat = mn * (1.0 / (1.0 - ADAM_B1 ** ADAM_STEP))
    v_hat = vn * (1.0 / (1.0 - ADAM_B2 ** ADAM_STEP))
    d_ref[...] = -ADAM_LR * (m_hat / (jnp.sqrt(v_hat) + ADAM_EPS) + ADAM_WD * w_ref[...])


def _adamw_call(w, g, m, v, name):
    R, C = w.shape
    tr = _row_tile(R, C)

    def body(w_ref, g_ref, m_ref, v_ref, go_ref, d_ref, mo_ref, vo_ref):
        go_ref[...] = g_ref[...]
        _adamw(w_ref, g_ref, m_ref, v_ref, d_ref, mo_ref, vo_ref)

    spec = pl.BlockSpec((tr, C), lambda i: (i, 0))
    return pl.pallas_call(
        body, name=name, grid=(R // tr,),
        in_specs=[spec] * 4, out_specs=[spec] * 4,
        out_shape=[jax.ShapeDtypeStruct((R, C), F32)] * 4,
        compiler_params=_params(("parallel",)),
    )(w, g, m, v)


def _adamw_small_call(ws, gs, ms, vs):
    n = len(ws)

    def body(*refs):
        w_refs, g_refs, m_refs, v_refs, d_refs, mo_refs, vo_refs = (refs[i * n:(i + 1) * n] for i in range(7))
        for i in range(n):
            _adamw(w_refs[i], g_refs[i], m_refs[i], v_refs[i], d_refs[i], mo_refs[i], vo_refs[i])

    whole = pl.BlockSpec(memory_space=pltpu.VMEM)
    outs = pl.pallas_call(
        body, name="adamw_small",
        in_specs=[whole] * (4 * n), out_specs=[whole] * (3 * n),
        out_shape=[jax.ShapeDtypeStruct(w.shape, F32) for w in ws] * 3,
    )(*ws, *gs, *ms, *vs)
    return outs[:n], outs[n:2 * n], outs[2 * n:]


def _position():
    return lax.axis_index("x"), lax.axis_index("y"), lax.axis_index("c")


def _chip_peers(x, y):
    return [(x, 1 - y), (1 - x, y), (1 - x, 1 - y)]


def _half_rows(ref, core, rows):
    h = rows // 2
    start = pl.multiple_of(core * h, 16)
    return ref.at[pl.ds(start, h), :] if len(ref.shape) == 2 else ref.at[:, pl.ds(start, h), :]


def _shard_half(ref, shard, core, rows):
    h = rows // 2
    return ref.at[shard, pl.ds(pl.multiple_of(core * h, 16), h), :]


ANY = pl.BlockSpec(memory_space=pl.ANY)


def _first_hop_copies(srcs, lands):
    x, y, c = _position()
    chip = 2 * x + y
    targets = [(px, py, c) for px, py in _chip_peers(x, y)] + [(x, y, 1 - c)]
    rows = srcs[0].shape[0]
    out = []
    for i, (s, l) in enumerate(zip(srcs, lands)):
        for k, dev in enumerate(targets):
            if i == 0 and k < 3:
                out.append((_half_rows(s, c, rows), _shard_half(l, chip, c, rows), dev, k))
            else:
                out.append((s, l.at[chip], dev, len(targets) * i + k))
    return out


def _second_hop_copies(srcs, lands):
    x, y, c = _position()
    rows = lands[0].shape[1]
    out = []
    for k, (px, py) in enumerate(_chip_peers(x, y)):
        half = _shard_half(lands[0], 2 * px + py, c, rows)
        out.append((half, half, (x, y, 1 - c), k))
    return out


HBM_SPEC = pl.BlockSpec(memory_space=pltpu.HBM)
SEM_SPEC = pl.BlockSpec(memory_space=pltpu.SEMAPHORE)
DATAFLOW = pltpu.SideEffectType.DATAFLOW_SIDE_EFFECTING


def _in_hbm(a):
    return pltpu.with_memory_space_constraint(a, pltpu.HBM)


def _split_start(name, groups, after, carry=None):
    spans, arrays = [], []
    for srcs, lands, _, _ in groups:
        spans.append((len(arrays), len(srcs), len(lands)))
        arrays += list(srcs) + list(lands)
    if carry is not None:
        arrays.append(carry)
    na, ng = len(arrays), len(groups)

    def body(*refs):
        sems, token = refs[na + 1:na + 1 + 2 * ng], refs[-1]
        for g, (_, _, _, copies) in enumerate(groups):
            off, ns, nl = spans[g]
            for src, dst, dev, idx in copies(refs[off:off + ns], refs[off + ns:off + ns + nl]):
                pltpu.make_async_remote_copy(src_ref=src, dst_ref=dst, send_sem=sems[2 * g].at[idx], recv_sem=sems[2 * g + 1].at[idx],
                                             device_id=dev, device_id_type=MESH).start()
        token[...] = jnp.zeros_like(token)

    outs = pl.pallas_call(
        body, name=name,
        in_specs=[HBM_SPEC] * na + [ANY],
        out_specs=[SEM_SPEC] * (2 * ng) + [HBM_SPEC] * na + [pl.BlockSpec(memory_space=pltpu.VMEM)],
        out_shape=[pltpu.SemaphoreType.DMA((n_sems,)) for _, _, n_sems, _ in groups for _ in range(2)]
        + [pltpu.HBM(a.shape, a.dtype) for a in arrays] + [jax.ShapeDtypeStruct((SUBLANES, LANES), F32)],
        input_output_aliases={i: 2 * ng + i for i in range(na)},
        compiler_params=pltpu.CompilerParams(has_side_effects=DATAFLOW),
    )(*[_in_hbm(a) for a in arrays], after)
    started = []
    for g, (off, ns, nl) in enumerate(spans):
        thru = outs[2 * ng + off:2 * ng + off + ns + nl]
        started.append(dict(send=outs[2 * g], recv=outs[2 * g + 1], srcs=list(thru[:ns]), lands=list(thru[ns:]),
                            tile=outs[-1], token=outs[-1][0, 0], carry=None if carry is None else outs[2 * ng + na - 1]))
    return started


def _split_wait(name, started, copies, after):
    n, m = len(started["srcs"]), len(started["lands"])

    def body(*refs):
        src_refs, land_refs = refs[:n], refs[n:n + m]
        send_sem, recv_sem = refs[n + m], refs[n + m + 1]
        for src, dst, dev, idx in copies(src_refs, land_refs):
            cp = pltpu.make_async_remote_copy(src_ref=src, dst_ref=dst, send_sem=send_sem.at[idx], recv_sem=recv_sem.at[idx],
                                              device_id=dev, device_id_type=MESH)
            cp.wait_send()
            cp.wait_recv()

    arrays = started["srcs"] + started["lands"]
    outs = pl.pallas_call(
        body, name=name,
        in_specs=[HBM_SPEC] * (n + m) + [SEM_SPEC, SEM_SPEC, ANY],
        out_specs=[HBM_SPEC] * (n + m),
        out_shape=[pltpu.HBM(a.shape, a.dtype) for a in arrays],
        input_output_aliases={i: i for i in range(n + m)},
        compiler_params=pltpu.CompilerParams(has_side_effects=DATAFLOW),
    )(*arrays, started["send"], started["recv"], after)
    return list(outs)


def _gather_copies(srcs, lands):
    x, y, c = _position()
    chip = 2 * x + y
    targets = [(px, py, c) for px, py in _chip_peers(x, y)] + [(x, y, 1 - c)]
    return [(s, l.at[chip], dev, len(targets) * i + k) for i, (s, l) in enumerate(zip(srcs, lands)) for k, dev in enumerate(targets)]


def _sibling_copies(srcs, lands):
    x, y, c = _position()
    return [(_half_rows(srcs[0], 1 - c, srcs[0].shape[1]), lands[0], (x, y, 1 - c), 0)]


def _exchange_copies(srcs, lands):
    x, y, c = _position()
    return [(srcs[0].at[2 * px + py], lands[0].at[k], (px, py, c), k) for k, (px, py) in enumerate(_chip_peers(x, y))]


def _pair_sum_call(grad, recv, core, name):
    _, h, B = recv.shape
    tr = _row_tile(h, B)

    def body(core_ref, g_ref, r_ref, o_ref, ob_ref):
        s = g_ref[...] + r_ref[...]
        o_ref[...] = s
        ob_ref[...] = s.astype(BF16)

    g_spec = pl.BlockSpec((None, tr, B), lambda q, i, core_ref: (q, core_ref[0] * (h // tr) + i, 0))
    spec = pl.BlockSpec((None, tr, B), lambda q, i, core_ref: (q, i, 0))
    return pl.pallas_call(
        body, name=name,
        grid_spec=pltpu.PrefetchScalarGridSpec(num_scalar_prefetch=1, grid=(N_CHIPS, h // tr), in_specs=[g_spec, spec],
                                               out_specs=[spec, spec]),
        out_shape=[jax.ShapeDtypeStruct(recv.shape, F32), jax.ShapeDtypeStruct(recv.shape, BF16)],
        compiler_params=_params(("parallel", "parallel")),
    )(core, grad, recv)


def _chip_sum_call(partial, recv, chip_core, name):
    _, h, B = recv.shape
    tr = _row_tile(h, B)

    def body(cc_ref, p_ref, r_ref, o_ref):
        o_ref[...] = ((p_ref[...] + r_ref[0].astype(F32)) + r_ref[1].astype(F32)) + r_ref[2].astype(F32)

    return pl.pallas_call(
        body, name=name,
        grid_spec=pltpu.PrefetchScalarGridSpec(
            num_scalar_prefetch=1, grid=(h // tr,),
            in_specs=[pl.BlockSpec((None, tr, B), lambda i, cc_ref: (cc_ref[0], i, 0)),
                      pl.BlockSpec((3, tr, B), lambda i, cc_ref: (0, i, 0))],
            out_specs=pl.BlockSpec((tr, B), lambda i, cc_ref: (cc_ref[1] * (h // tr) + i, 0))),
        out_shape=jax.ShapeDtypeStruct((2 * h, B), F32),
        compiler_params=_params(("parallel",)),
    )(chip_core, partial, recv)


def _sibling_assemble_call(shards, name="grad_sibling_assemble"):
    n = len(shards)

    def body(*refs):
        ins, outs = refs[:n], refs[n:2 * n]
        send_sems, recv_sems = refs[2 * n:]
        x, y, c = _position()
        copies = []
        for i in range(n):
            rows = shards[i].shape[0]
            cp = pltpu.make_async_remote_copy(src_ref=_half_rows(ins[i], c, rows), dst_ref=_half_rows(outs[i], c, rows),
                                              send_sem=send_sems.at[i], recv_sem=recv_sems.at[i],
                                              device_id=(x, y, 1 - c), device_id_type=MESH)
            cp.start()
            copies.append(cp)
        for cp in copies:
            cp.wait()

    return pl.pallas_call(
        body, name=name,
        in_specs=[ANY] * n, out_specs=[ANY] * n,
        out_shape=[jax.ShapeDtypeStruct(s.shape, F32) for s in shards],
        input_output_aliases={i: i for i in range(n)},
        scratch_shapes=[pltpu.SemaphoreType.DMA((n,)), pltpu.SemaphoreType.DMA((n,))],
    )(*shards)


N_DEVICES = 8


def _allsum_copies(srcs, lands):
    x, y, c = _position()
    me = 4 * x + 2 * y + c
    out = []
    for k in range(1, N_DEVICES):
        peer = (1 - x if k & 4 else x, 1 - y if k & 2 else y, 1 - c if k & 1 else c)
        out.append((srcs[0], lands[0].at[me], peer, k - 1))
    return out


def _ordered_sum_call(mine, landed, me_chip, shapes, sharded_cols):
    rows = mine.shape[0]
    outs = [(s[0], n) if n else s for s, n in zip(shapes, sharded_cols)]

    def body(mc_ref, x_ref, l_ref, *refs):
        acc_ref = refs[-1]
        acc = jnp.where(mc_ref[0] == 0, x_ref[...], l_ref[0])
        for d in range(1, N_DEVICES):
            acc = acc + jnp.where(mc_ref[0] == d, x_ref[...], l_ref[d])
        acc_ref[...] = acc
        first = 0
        for o_ref, (r, c), n in zip(refs[:-1], shapes, sharded_cols):
            per_row = c // LANES

            def unpack(chip, o_ref=o_ref, r=r, n=n, per_row=per_row, first=first):
                for i in range(r):
                    for j in range((n or per_row * LANES) // LANES):
                        src = first + i * per_row + chip * ((n or 0) // LANES) + j
                        o_ref[i:i + 1, j * LANES:(j + 1) * LANES] = acc_ref[src:src + 1, :]

            if n:
                for q in range(N_CHIPS):
                    pl.when(mc_ref[1] == q)(functools.partial(unpack, q))
            else:
                unpack(0)
            first += r * per_row

    results = pl.pallas_call(
        body, name="small_grad_sum",
        in_specs=[pl.BlockSpec(memory_space=pltpu.SMEM), pl.BlockSpec(memory_space=pltpu.VMEM), pl.BlockSpec(memory_space=pltpu.VMEM)],
        out_specs=[pl.BlockSpec(memory_space=pltpu.VMEM)] * len(outs),
        out_shape=[jax.ShapeDtypeStruct(s, F32) for s in outs],
        scratch_shapes=[pltpu.VMEM((rows, LANES), F32)],
    )(me_chip, mine, landed)
    return results


def _pack(arrays):
    flat = jnp.concatenate([a.reshape(-1).astype(F32) for a in arrays])
    rows = -(-flat.shape[0] // LANES)
    rows = -(-rows // SUBLANES) * SUBLANES
    flat = jnp.pad(flat, (0, rows * LANES - flat.shape[0]))
    return flat.reshape(rows, LANES)


def _local_step(xs, target, P, late_weights, on_grad):
    S, D = xs.shape
    qkv_width = 3 * N_HEADS * HEAD_DIM
    glu_col0, gate_col0 = qkv_width, qkv_width + 2 * D
    shard_major = lambda g: g.reshape(N_CHIPS, g.shape[0] // N_CHIPS, g.shape[1])

    h1 = _rms_fwd_call(xs, P["norm_mix_pre"])
    buckets = _bucket_tables()
    bias = _bias_table_call(P["rel_bias"] + 0.0 * h1[0, 0].astype(F32), buckets)
    P = dict(P, **late_weights("in", bias))
    proj = _matmul(h1, P["w_in"], "nn", "proj_in")
    parts = []
    for g in range(N_GROUPS):
        parts += _attn_fwd_call(proj, bias, g)
    a, a_bf, lse = _attn_merge_call(parts)
    P = dict(P, **late_weights("mix", a_bf))
    y_a = _matmul(a_bf, P["w_attn_out"], "nn", "attn_out")
    c1 = _conv_fwd_call(proj, glu_col0, P["conv_dw_w"], P["conv_dw_b"])
    cact = _ln_silu_call(c1, P["conv_ln_g"], P["conv_ln_b"])
    y_c = _matmul(cact, P["conv_pw_w"], "nn", "conv_pw")
    mixed = _mix_call(proj, gate_col0, P["b_gate"], y_a, y_c)
    out = _matmul(mixed, P["w_out"], "nn", "mix_out")
    x1, h2 = _res1_call(xs, out, P["norm_mix_post"], P["norm_ffn_pre"])
    P = dict(P, **late_weights("up", h2))
    u = _matmul(h2, P["w_up"], "nn", "ffn_up")
    f = _ffn_fwd_call(u, P["ffn_conv_w"], P["ffn_conv_b"])
    P = dict(P, **late_weights("down", f))
    yff = _matmul(f, P["w_down"], "nn", "ffn_down")
    loss_tile, dx2, dyff, dg_ffn_post = _loss_call(yff, x1, P["norm_ffn_post"], target)

    G = {}
    G["norm_ffn_post"] = dg_ffn_post
    zero = on_grad("w_down", shard_major(_matmul(f, dyff, "tn", "ffn_down_dw")))
    df = _matmul(dyff, P["w_down"], "nt", "ffn_down_dx")
    du, dwg, dwv, dbg, dbv = _ffn_bwd_call(u, P["ffn_conv_w"], P["ffn_conv_b"] + zero, df)
    G["ffn_conv_w"] = jnp.concatenate([dwg[:FFN_CONV_WIDTH], dwv[:FFN_CONV_WIDTH]], axis=1)
    G["ffn_conv_b"] = jnp.concatenate([dbg, dbv], axis=1)
    zero = on_grad("w_up", _matmul(h2, du, "tn", "ffn_up_dw", out_shards=True))
    dh2 = _matmul(du, P["w_up"], "nt", "ffn_up_dx")
    dx1, dout, G["norm_ffn_pre"], G["norm_mix_post"] = _mid_bwd_call(x1, P["norm_ffn_pre"] + zero, dh2, dx2, out, P["norm_mix_post"])
    zero = on_grad("w_out", shard_major(_matmul(mixed, dout, "tn", "mix_out_dw")))
    dmixed = _matmul(dout, P["w_out"], "nt", "mix_out_dx")
    dya, dyc, dga, dgc, dba, dbc = _mix_bwd_call(dmixed, proj, gate_col0, P["b_gate"] + zero, y_a, y_c)
    G["b_gate"] = jnp.concatenate([dba, dbc], axis=1)
    zero = on_grad("w_attn_out", _matmul(a_bf, dya, "tn", "attn_out_dw", out_shards=True))
    zero = zero + on_grad("conv_pw_w", shard_major(_matmul(cact, dyc, "tn", "conv_pw_dw")))
    da = _matmul(dya, P["w_attn_out"], "nt", "attn_out_dx")
    dcact = _matmul(dyc, P["conv_pw_w"], "nt", "conv_pw_dx")
    dc1, G["conv_ln_g"], G["conv_ln_b"] = _ln_silu_bwd_call(c1, P["conv_ln_g"] + zero, P["conv_ln_b"], dcact)
    dval, dgate, dw_dw, G["conv_dw_b"] = _conv_bwd_call(proj, glu_col0, P["conv_dw_w"], dc1)
    G["conv_dw_w"] = dw_dw[:CONV_WIDTH]
    delta = _attn_delta_call(a, da)
    dqs, dks, dvs, dbs = [], [], [], []
    for g in range(N_GROUPS):
        dq, dk, dv, db = _attn_bwd_call(proj, bias, da, lse, delta, g)
        dqs.append(dq)
        dks.append(dk)
        dvs.append(dv)
        dbs.append(db)
    G["rel_bias"] = _bias_grad_call(jnp.concatenate(dbs, axis=0), buckets)
    dproj = _dproj_call(dqs + dks + dvs, [dval, dgate, dga, dgc])
    zero, dproj = on_grad("w_in", _matmul(h1, dproj, "tn", "proj_in_dw", out_shards=True), carry=dproj)
    dh1 = _matmul(dproj, P["w_in"], "nt", "proj_in_dx")
    zero = zero + on_grad(None, dh1)
    grad_x, G["norm_mix_pre"] = _in_bwd_call(xs, P["norm_mix_pre"] + zero, dh1, dx1)
    return loss_tile, grad_x, G


def kernel(x, w_in, b_gate, rel_bias, w_attn_out, conv_dw_w, conv_dw_b, conv_ln_g, conv_ln_b, conv_pw_w, w_out, norm_mix_pre, norm_mix_post, norm_ffn_pre, norm_ffn_post, w_up, ffn_conv_w, ffn_conv_b, w_down, loss_target, m_w_in, m_b_gate, m_rel_bias, m_w_attn_out, m_conv_dw_w, m_conv_dw_b, m_conv_ln_g, m_conv_ln_b, m_conv_pw_w, m_w_out, m_norm_mix_pre, m_norm_mix_post, m_norm_ffn_pre, m_norm_ffn_post, m_w_up, m_ffn_conv_w, m_ffn_conv_b, m_w_down, v_w_in, v_b_gate, v_rel_bias, v_w_attn_out, v_conv_dw_w, v_conv_dw_b, v_conv_ln_g, v_conv_ln_b, v_conv_pw_w, v_w_out, v_norm_mix_pre, v_norm_mix_post, v_norm_ffn_pre, v_norm_ffn_post, v_w_up, v_ffn_conv_w, v_ffn_conv_b, v_w_down):
    weights = dict(w_in=w_in, b_gate=b_gate, rel_bias=rel_bias, w_attn_out=w_attn_out, conv_dw_w=conv_dw_w, conv_dw_b=conv_dw_b,
                   conv_ln_g=conv_ln_g, conv_ln_b=conv_ln_b, conv_pw_w=conv_pw_w, w_out=w_out, norm_mix_pre=norm_mix_pre,
                   norm_mix_post=norm_mix_post, norm_ffn_pre=norm_ffn_pre, norm_ffn_post=norm_ffn_post, w_up=w_up,
                   ffn_conv_w=ffn_conv_w, ffn_conv_b=ffn_conv_b, w_down=w_down)
    m_in = dict(w_in=m_w_in, b_gate=m_b_gate, rel_bias=m_rel_bias, w_attn_out=m_w_attn_out, conv_dw_w=m_conv_dw_w,
                conv_dw_b=m_conv_dw_b, conv_ln_g=m_conv_ln_g, conv_ln_b=m_conv_ln_b, conv_pw_w=m_conv_pw_w, w_out=m_w_out,
                norm_mix_pre=m_norm_mix_pre, norm_mix_post=m_norm_mix_post, norm_ffn_pre=m_norm_ffn_pre,
                norm_ffn_post=m_norm_ffn_post, w_up=m_w_up, ffn_conv_w=m_ffn_conv_w, ffn_conv_b=m_ffn_conv_b, w_down=m_w_down)
    v_in = dict(w_in=v_w_in, b_gate=v_b_gate, rel_bias=v_rel_bias, w_attn_out=v_w_attn_out, conv_dw_w=v_conv_dw_w,
                conv_dw_b=v_conv_dw_b, conv_ln_g=v_conv_ln_g, conv_ln_b=v_conv_ln_b, conv_pw_w=v_conv_pw_w, w_out=v_w_out,
                norm_mix_pre=v_norm_mix_pre, norm_mix_post=v_norm_mix_post, norm_ffn_pre=v_norm_ffn_pre,
                norm_ffn_post=v_norm_ffn_post, w_up=v_w_up, ffn_conv_w=v_ffn_conv_w, ffn_conv_b=v_ffn_conv_b, w_down=v_w_down)
    names = list(weights)
    xi, yi, ci = _position()
    chip = 2 * xi + yi
    core_arr = jnp.reshape(ci, (1,)).astype(jnp.int32)

    xs = x[0]
    target = loss_target[0]
    S, D = xs.shape

    big = ["w_in", "w_attn_out", "conv_pw_w", "w_out", "w_up", "w_down"]
    row_sharded = ("conv_pw_w", "w_out", "w_down")
    natural = lambda k, g: g.reshape(-1, g.shape[2]) if k in row_sharded else g
    first_srcs = [w_in[0].astype(BF16), conv_dw_w[0], ffn_conv_w[0]]
    first_lands = [lax.empty((N_CHIPS,) + s.shape, s.dtype) for s in first_srcs]
    (first_hop,) = _split_start("gather_in_start", [(first_srcs, first_lands, 4 * len(first_srcs), _first_hop_copies)], core_arr)
    launched = first_hop["token"]
    late_sets = dict(mix=["w_attn_out", "conv_pw_w", "w_out"], up=["w_up"], down=["w_down"])
    late_groups = []
    for keys in late_sets.values():
        srcs = [(weights[k][0] + launched).astype(BF16) for k in keys]
        late_groups.append((srcs, [lax.empty((N_CHIPS,) + s.shape, BF16) for s in srcs], 4 * len(keys), _gather_copies))
    started = {}

    def late_weights(tag, after):
        if tag == "in":
            w_in_halves, dw4, fc4 = _split_wait("gather_in_wait", first_hop, _first_hop_copies, after)[len(first_srcs):]
            second_hop, *late = _split_start("gather_in_pass_start", [([], [w_in_halves], 3, _second_hop_copies)] + late_groups, dw4)
            started.update(zip(late_sets, late))
            (w_in_full,) = _split_wait("gather_in_pass_wait", second_hop, _second_hop_copies, second_hop["tile"])
            return dict(w_in=w_in_full, conv_dw_w=jnp.concatenate(list(dw4), axis=1), ffn_conv_w=jnp.concatenate(list(fc4), axis=1))
        landed = _split_wait(f"gather_{tag}_wait", started[tag], _gather_copies, after)[len(late_sets[tag]):]
        return {k: natural(k, g) for k, g in zip(late_sets[tag], landed)}

    chip_core = jnp.stack([chip, ci]).astype(jnp.int32)
    exchanging, pending = {}, {}

    held = []

    def launch(tag, after, carry=None):
        keys, groups, partial = [], [], {}
        for k in list(exchanging):
            gk, r1 = _split_wait(f"sibling_exchange_wait_{k}", exchanging.pop(k), _sibling_copies, after)
            partial[k], s16 = _pair_sum_call(gk, r1, core_arr, f"pair_sum_{k}")
            keys.append(k)
            groups.append(([s16], [lax.empty((3,) + s16.shape[1:], BF16)], 3, _exchange_copies))
        fresh = [k for k, _ in held]
        for _, g3 in held:
            groups.append(([g3], [lax.empty((N_CHIPS, g3.shape[1] // 2, g3.shape[2]), F32)], 1, _sibling_copies))
        held.clear()
        begun = _split_start(f"grad_exchange_start_{tag}", groups, core_arr, carry)
        for k, st in zip(keys + fresh, begun):
            if k in partial:
                pending[k] = (partial[k], st)
            else:
                exchanging[k] = st
        return begun[0]["token"] if carry is None else (begun[0]["token"], begun[0]["carry"])

    def on_grad(k, g3, carry=None):
        if k is None:
            return launch("last", g3[:SUBLANES, :LANES])
        held.append((k, g3))
        if k in ("w_down", "w_out", "w_attn_out"):
            return jnp.float32(0.0)
        return launch(k, g3[0, :SUBLANES, :LANES], carry)

    def finish(keys, after, tag):
        halves = []
        for k in keys:
            s32, st = pending[k]
            recv2 = _split_wait(f"chip_exchange_wait_{k}", st, _exchange_copies, after)[1]
            halves.append(_chip_sum_call(s32, recv2, chip_core, f"chip_sum_{k}"))
        return dict(zip(keys, _sibling_assemble_call(halves, f"grad_sibling_assemble_{tag}")))

    P = dict(b_gate=b_gate, rel_bias=rel_bias, conv_dw_b=conv_dw_b, conv_ln_g=conv_ln_g, conv_ln_b=conv_ln_b,
             norm_mix_pre=norm_mix_pre + launched, norm_mix_post=norm_mix_post, norm_ffn_pre=norm_ffn_pre,
             norm_ffn_post=norm_ffn_post, ffn_conv_b=ffn_conv_b)
    loss_tile, grad_x, G = _local_step(xs, target, P, late_weights, on_grad)

    small = [k for k in names if k not in big]
    packed = _pack([loss_tile[:1]] + [G[k] for k in small])
    (allsum,) = _split_start("small_grad_allsum_start",
                             [([packed], [jnp.zeros((N_DEVICES,) + packed.shape, F32)], N_DEVICES - 1, _allsum_copies)], core_arr)

    reduced, grads, deltas, new_m, new_v = {}, {}, {}, {}, {}

    def update(keys):
        for k in keys:
            gk, d, mn, vn = _adamw_call(weights[k][0], reduced[k], m_in[k][0], v_in[k][0], f"adamw_{k}")
            grads[k], deltas[k], new_m[k], new_v[k] = gk[None], d[None], mn[None], vn[None]

    others = [k for k in big if k != "w_in"]
    reduced.update(finish(others, allsum["tile"], "others"))
    update(others)
    reduced.update(finish(["w_in"], deltas["w_up"], "w_in"))
    update(["w_in"])

    me_chip = jnp.stack([4 * xi + 2 * yi + ci, chip]).astype(jnp.int32)
    mine, landed = _split_wait("small_grad_allsum_wait", allsum, _allsum_copies, deltas["w_in"])
    piece_shapes = [(1, LANES)] + [(G[k].size // LANES, LANES) if k == "rel_bias" else G[k].shape for k in small]
    piece_cols = [0] + [weights[k].shape[2] if k in ("conv_dw_w", "ffn_conv_w") else 0 for k in small]
    loss_row, *summed = _ordered_sum_call(mine, landed, me_chip, piece_shapes, piece_cols)
    loss = loss_row[0, 0]
    for k, gsum in zip(small, summed):
        grads[k] = gsum.reshape(weights[k].shape)
    ds, mns, vns = _adamw_small_call([weights[k] for k in small], [grads[k] for k in small],
                                     [m_in[k] for k in small], [v_in[k] for k in small])
    deltas.update(zip(small, ds))
    new_m.update(zip(small, mns))
    new_v.update(zip(small, vns))

    return (loss, grad_x[None], *[grads[k] for k in names], *[deltas[k] for k in names],
            *[new_m[k] for k in names], *[new_v[k] for k in names])
```

```python
import functools
import math

import jax
import jax.numpy as jnp
import numpy as np
from jax import lax
from jax.experimental import pallas as pl
from jax.experimental.pallas import tpu as pltpu

F32 = jnp.float32
BF16 = jnp.bfloat16
MESH = pl.DeviceIdType.MESH

HEAD_DIM = 128
HEADS_PER_GROUP = 4
DILATED_PATTERNS = ((128, 1), (512, 4), (2048, 16))
N_GROUPS = 3
N_HEADS = N_GROUPS * HEADS_PER_GROUP
SPAN = 128
GROUP_WIDTH = HEADS_PER_GROUP * HEAD_DIM
CONV_WIDTH = 31
FFN_CONV_WIDTH = 3
N_BUCKETS = 32
MAX_DISTANCE = 2048
RMS_EPS = 1e-6
LN_EPS = 1e-5
NEG_INF = -1e30
ADAM_LR = 0.001
ADAM_B1 = 0.9
ADAM_B2 = 0.999
ADAM_EPS = 1e-08
ADAM_WD = 0.01
ADAM_STEP = 10

LANES = 128
SUBLANES = 8
ROW_TILE = 512
GATE_ROWS, GATE_COLS = 512, 512
TIME_BLOCK = 128
CONV_PAD = 32
FFN_PAD = 8
VMEM_LIMIT = 56 << 20


def _params(sem=None, vmem=None):
    kw = {}
    if sem is not None:
        kw["dimension_semantics"] = sem
    if vmem is not None:
        kw["vmem_limit_bytes"] = vmem
    return pltpu.CompilerParams(**kw)


def _pick(n, cands):
    for c in cands:
        if n % c == 0:
            return c
    return n


ELEMENTWISE_TILE_BYTES = 3 << 19


def _row_tile(rows, cols):
    for align in (16, SUBLANES):
        fits = [t for t in range(align, rows + 1, align) if rows % t == 0 and t * cols * 4 <= ELEMENTWISE_TILE_BYTES]
        if fits:
            return max(fits)
    return SUBLANES


N_CHIPS = 4
M_TILES = (1024, 1408, 512, 256, 128)
N_TILES = (1024, 512, 1408, 256, 128)
K_TILES = (2176, 2048, 1408, 1024, 512, 256, 128)


def _matmul(a, b, mode, name, out_shards=False, tm=None):
    assert a.dtype == BF16 and b.dtype == BF16, (name, a.dtype, b.dtype)
    b3 = b.ndim == 3
    tn = tk = None
    halves = None
    if mode == "nn":
        M, K = a.shape
        N = b.shape[-1] * (N_CHIPS if b3 else 1)
        tn = b.shape[-1] if b3 else None
    elif mode == "nt":
        if a.ndim == 3:
            halves = a.shape[2]
        M, K = a.shape[-2], a.shape[-1] * (a.shape[0] if a.ndim == 3 else 1)
        N = b.shape[-2]
        tk = b.shape[-1] if b3 else None
    else:
        if b3:
            halves = b.shape[2]
        K, M = a.shape
        N = b.shape[-1] * (b.shape[0] if b3 else 1)
        tn = N // N_CHIPS if out_shards else None
    tm = tm or _pick(M, M_TILES)
    tn = tn or _pick(N, N_TILES)
    tk = tk or _pick(K, K_TILES)
    nk = K // tk
    dn = {"nn": (((1,), (0,)), ((), ())), "nt": (((1,), (1,)), ((), ())), "tn": (((0,), (0,)), ((), ()))}[mode]

    def body(a_ref, b_ref, o_ref):
        if nk == 1:
            o_ref[...] = lax.dot_general(a_ref[...], b_ref[...], dn, preferred_element_type=F32)
        else:
            @pl.when(pl.program_id(2) == 0)
            def _():
                o_ref[...] = jnp.zeros_like(o_ref)

            o_ref[...] += lax.dot_general(a_ref[...], b_ref[...], dn, preferred_element_type=F32)

    if mode == "tn":
        a_spec = pl.BlockSpec((tk, tm), lambda i, j, k: (k, i))
    elif halves:
        per = halves // tk
        a_spec = pl.BlockSpec((None, tm, tk), lambda i, j, k: (k // per, i, k % per))
    else:
        a_spec = pl.BlockSpec((tm, tk), lambda i, j, k: (i, k))
    if mode == "nn":
        b_spec = pl.BlockSpec((None, tk, tn), lambda i, j, k: (j, k, 0)) if b3 else pl.BlockSpec((tk, tn), lambda i, j, k: (k, j))
    elif mode == "nt":
        b_spec = pl.BlockSpec((None, tn, tk), lambda i, j, k: (k, j, 0)) if b3 else pl.BlockSpec((tn, tk), lambda i, j, k: (j, k))
    elif halves:
        per = halves // tn
        b_spec = pl.BlockSpec((None, tk, tn), lambda i, j, k: (j // per, k, j % per))
    else:
        b_spec = pl.BlockSpec((tk, tn), lambda i, j, k: (k, j))
    if out_shards:
        out_spec = pl.BlockSpec((None, tm, tn), lambda i, j, k: (j, i, 0))
        out_shape = jax.ShapeDtypeStruct((N_CHIPS, M, tn), F32)
    else:
        out_spec = pl.BlockSpec((tm, tn), lambda i, j, k: (i, j))
        out_shape = jax.ShapeDtypeStruct((M, N), F32)
    return pl.pallas_call(
        body, name=name, grid=(M // tm, N // tn, nk),
        in_specs=[a_spec, b_spec], out_specs=out_spec, out_shape=out_shape,
        compiler_params=_params(("parallel", "parallel", "arbitrary"), VMEM_LIMIT),
    )(a, b)


def _rms(x, g):
    r = lax.rsqrt(jnp.mean(x * x, axis=-1, keepdims=True) + RMS_EPS)
    return x * r * g


def _rms_bwd(x, g, dy):
    r = lax.rsqrt(jnp.mean(x * x, axis=-1, keepdims=True) + RMS_EPS)
    n = x * r
    dn = dy * g
    dx = r * (dn - n * jnp.mean(dn * n, axis=-1, keepdims=True))
    return dx, jnp.sum(dy * n, axis=0, keepdims=True)


def _sigmoid(x):
    return 1.0 / (1.0 + jnp.exp(-x))


_GELU_C = math.sqrt(2.0 / math.pi)


def _gelu(x):
    return 0.5 * x * (1.0 + jnp.tanh(_GELU_C * (x + 0.044715 * x * x * x)))


def _gelu_and_grad(x):
    x2 = x * x
    t = jnp.tanh(_GELU_C * x * (1.0 + 0.044715 * x2))
    half = 0.5 * (1.0 + t)
    return x * half, half + (0.5 * _GELU_C) * x * (1.0 - t * t) * (1.0 + (3.0 * 0.044715) * x2)


def _row_spec(width, col_block=0):
    return pl.BlockSpec((ROW_TILE, width), lambda i: (i, col_block))


def _vec_spec(width, col_block=0):
    return pl.BlockSpec((1, width), lambda i: (0, col_block))


def _accumulate(ref, part):
    @pl.when(pl.program_id(0) == 0)
    def _():
        ref[...] = part

    @pl.when(pl.program_id(0) > 0)
    def _():
        ref[...] += part


def _rms_fwd_call(x, g):
    S, D = x.shape

    def body(x_ref, g_ref, h_ref):
        h_ref[...] = _rms(x_ref[...], g_ref[...]).astype(BF16)

    return pl.pallas_call(
        body, name="rms_mix_pre", grid=(S // ROW_TILE,),
        in_specs=[_row_spec(D), _vec_spec(D)], out_specs=_row_spec(D),
        out_shape=jax.ShapeDtypeStruct((S, D), BF16),
        compiler_params=_params(("parallel",)),
    )(x, g)


def _ln_silu_call(c1, g, b):
    S, C = c1.shape

    def body(c_ref, g_ref, b_ref, o_ref):
        xv = c_ref[...]
        mu = jnp.mean(xv, axis=-1, keepdims=True)
        xc = xv - mu
        var = jnp.mean(xc * xc, axis=-1, keepdims=True)
        z = xc * lax.rsqrt(var + LN_EPS) * g_ref[...] + b_ref[...]
        o_ref[...] = (z * _sigmoid(z)).astype(BF16)

    return pl.pallas_call(
        body, name="conv_ln_silu", grid=(S // ROW_TILE,),
        in_specs=[_row_spec(C), _vec_spec(C), _vec_spec(C)], out_specs=_row_spec(C),
        out_shape=jax.ShapeDtypeStruct((S, C), BF16),
        compiler_params=_params(("parallel",)),
    )(c1, g, b)


def _ln_silu_bwd_call(c1, g, b, dc):
    S, C = c1.shape

    def body(c_ref, g_ref, b_ref, dc_ref, dx_ref, dg_ref, db_ref):
        xv = c_ref[...]
        mu = jnp.mean(xv, axis=-1, keepdims=True)
        xc = xv - mu
        rs = lax.rsqrt(jnp.mean(xc * xc, axis=-1, keepdims=True) + LN_EPS)
        xh = xc * rs
        z = xh * g_ref[...] + b_ref[...]
        sg = _sigmoid(z)
        dz = dc_ref[...] * (sg * (1.0 + z * (1.0 - sg)))
        dxh = dz * g_ref[...]
        dx_ref[...] = rs * (dxh - jnp.mean(dxh, axis=-1, keepdims=True) - xh * jnp.mean(dxh * xh, axis=-1, keepdims=True))
        _accumulate(dg_ref, jnp.sum(dz * xh, axis=0, keepdims=True))
        _accumulate(db_ref, jnp.sum(dz, axis=0, keepdims=True))

    return pl.pallas_call(
        body, name="conv_ln_silu_bwd", grid=(S // ROW_TILE,),
        in_specs=[_row_spec(C), _vec_spec(C), _vec_spec(C), _row_spec(C)],
        out_specs=[_row_spec(C), _vec_spec(C), _vec_spec(C)],
        out_shape=[jax.ShapeDtypeStruct((S, C), F32), jax.ShapeDtypeStruct((1, C), F32), jax.ShapeDtypeStruct((1, C), F32)],
        compiler_params=_params(("arbitrary",)),
    )(c1, g, b, dc)


def _mix_call(proj, gate_col0, b_gate, y_a, y_c):
    S, D = y_a.shape
    w = GATE_COLS
    nc = D // w
    ga0, gc0 = gate_col0 // w, (gate_col0 + D) // w

    def body(ga_ref, gc_ref, ba_ref, bc_ref, ya_ref, yc_ref, o_ref):
        o_ref[...] = (_sigmoid(ga_ref[...] + ba_ref[...]) * ya_ref[...]
                      + _sigmoid(gc_ref[...] + bc_ref[...]) * yc_ref[...]).astype(BF16)

    tile = lambda off: pl.BlockSpec((GATE_ROWS, w), lambda i, j: (i, off + j))
    vec = lambda off: pl.BlockSpec((1, w), lambda i, j: (0, off + j))
    return pl.pallas_call(
        body, name="gate_mix", grid=(S // GATE_ROWS, nc),
        in_specs=[tile(ga0), tile(gc0), vec(0), vec(nc), tile(0), tile(0)],
        out_specs=tile(0), out_shape=jax.ShapeDtypeStruct((S, D), BF16),
        compiler_params=_params(("parallel", "parallel")),
    )(proj, proj, b_gate, b_gate, y_a, y_c)


def _mix_bwd_call(dmixed, proj, gate_col0, b_gate, y_a, y_c):
    S, D = y_a.shape
    w = GATE_COLS
    nc = D // w
    ga0, gc0 = gate_col0 // w, (gate_col0 + D) // w

    def body(dm_ref, ga_ref, gc_ref, ba_ref, bc_ref, ya_ref, yc_ref, dya_ref, dyc_ref, dga_ref, dgc_ref, dba_ref, dbc_ref):
        dm = dm_ref[...]
        sa = _sigmoid(ga_ref[...] + ba_ref[...])
        sc = _sigmoid(gc_ref[...] + bc_ref[...])
        dya_ref[...] = (dm * sa).astype(BF16)
        dyc_ref[...] = (dm * sc).astype(BF16)
        dga = dm * ya_ref[...] * sa * (1.0 - sa)
        dgc = dm * yc_ref[...] * sc * (1.0 - sc)
        dga_ref[...] = dga.astype(BF16)
        dgc_ref[...] = dgc.astype(BF16)
        pa = jnp.sum(dga, axis=0, keepdims=True)
        pc = jnp.sum(dgc, axis=0, keepdims=True)

        @pl.when(pl.program_id(1) == 0)
        def _():
            dba_ref[...] = pa
            dbc_ref[...] = pc

        @pl.when(pl.program_id(1) > 0)
        def _():
            dba_ref[...] += pa
            dbc_ref[...] += pc

    tile = lambda off: pl.BlockSpec((GATE_ROWS, w), lambda j, i: (i, off + j))
    vec = lambda off: pl.BlockSpec((1, w), lambda j, i: (0, off + j))
    return pl.pallas_call(
        body, name="gate_mix_bwd", grid=(nc, S // GATE_ROWS),
        in_specs=[tile(0), tile(ga0), tile(gc0), vec(0), vec(nc), tile(0), tile(0)],
        out_specs=[tile(0), tile(0), tile(0), tile(0), vec(0), vec(0)],
        out_shape=[jax.ShapeDtypeStruct((S, D), BF16)] * 4 + [
                   jax.ShapeDtypeStruct((1, D), F32), jax.ShapeDtypeStruct((1, D), F32)],
        compiler_params=_params(("parallel", "arbitrary")),
    )(dmixed, proj, proj, b_gate, b_gate, y_a, y_c)


def _res1_call(x, out, g_post, g_pre):
    S, D = x.shape

    def body(x_ref, o_ref, gp_ref, gq_ref, x1_ref, h2_ref):
        x1 = x_ref[...] + _rms(o_ref[...], gp_ref[...])
        x1_ref[...] = x1
        h2_ref[...] = _rms(x1, gq_ref[...]).astype(BF16)

    return pl.pallas_call(
        body, name="residual_mix", grid=(S // ROW_TILE,),
        in_specs=[_row_spec(D), _row_spec(D), _vec_spec(D), _vec_spec(D)],
        out_specs=[_row_spec(D), _row_spec(D)],
        out_shape=[jax.ShapeDtypeStruct((S, D), F32), jax.ShapeDtypeStruct((S, D), BF16)],
        compiler_params=_params(("parallel",)),
    )(x, out, g_post, g_pre)


def _loss_call(y, x1, g_post, target):
    S, D = y.shape

    def body(y_ref, x1_ref, g_ref, t_ref, loss_ref, dx_ref, dy_ref, dg_ref):
        yv, gv = y_ref[...], g_ref[...]
        err = x1_ref[...] + _rms(yv, gv) - t_ref[...]
        dx2 = err * (1.0 / D)
        dx_ref[...] = dx2
        dy, dg = _rms_bwd(yv, gv, dx2)
        dy_ref[...] = dy.astype(BF16)
        _accumulate(dg_ref, dg)
        part = 0.5 * jnp.sum(jnp.mean(err * err, axis=-1, keepdims=True), axis=0, keepdims=True)
        _accumulate(loss_ref, jnp.broadcast_to(part, (SUBLANES, LANES)))

    return pl.pallas_call(
        body, name="residual_ffn_loss", grid=(S // ROW_TILE,),
        in_specs=[_row_spec(D), _row_spec(D), _vec_spec(D), _row_spec(D)],
        out_specs=[pl.BlockSpec((SUBLANES, LANES), lambda i: (0, 0)), _row_spec(D), _row_spec(D), _vec_spec(D)],
        out_shape=[jax.ShapeDtypeStruct((SUBLANES, LANES), F32), jax.ShapeDtypeStruct((S, D), F32),
                   jax.ShapeDtypeStruct((S, D), BF16), jax.ShapeDtypeStruct((1, D), F32)],
        compiler_params=_params(("arbitrary",)),
    )(y, x1, g_post, target)


def _mid_bwd_call(x1, g_pre, dh2, dx2, out, g_post):
    S, D = x1.shape

    def body(x1_ref, gq_ref, dh_ref, dx2_ref, o_ref, gp_ref, dx1_ref, do_ref, dgq_ref, dgp_ref):
        d, dgq = _rms_bwd(x1_ref[...], gq_ref[...], dh_ref[...])
        dx1 = dx2_ref[...] + d
        dx1_ref[...] = dx1
        do, dgp = _rms_bwd(o_ref[...], gp_ref[...], dx1)
        do_ref[...] = do.astype(BF16)
        _accumulate(dgq_ref, dgq)
        _accumulate(dgp_ref, dgp)

    return pl.pallas_call(
        body, name="residual_mix_bwd", grid=(S // ROW_TILE,),
        in_specs=[_row_spec(D), _vec_spec(D), _row_spec(D), _row_spec(D), _row_spec(D), _vec_spec(D)],
        out_specs=[_row_spec(D), _row_spec(D), _vec_spec(D), _vec_spec(D)],
        out_shape=[jax.ShapeDtypeStruct((S, D), F32), jax.ShapeDtypeStruct((S, D), BF16)] + [jax.ShapeDtypeStruct((1, D), F32)] * 2,
        compiler_params=_params(("arbitrary",)),
    )(x1, g_pre, dh2, dx2, out, g_post)


def _in_bwd_call(x, g, dh1, dx1):
    S, D = x.shape

    def body(x_ref, g_ref, dh_ref, dx1_ref, gx_ref, dg_ref):
        d, dg = _rms_bwd(x_ref[...], g_ref[...], dh_ref[...])
        gx_ref[...] = dx1_ref[...] + d
        _accumulate(dg_ref, dg)

    return pl.pallas_call(
        body, name="rms_mix_pre_bwd", grid=(S // ROW_TILE,),
        in_specs=[_row_spec(D), _vec_spec(D), _row_spec(D), _row_spec(D)],
        out_specs=[_row_spec(D), _vec_spec(D)],
        out_shape=[jax.ShapeDtypeStruct((S, D), F32), jax.ShapeDtypeStruct((1, D), F32)],
        compiler_params=_params(("arbitrary",)),
    )(x, g, dh1, dx1)


def _bucket_table(dilation):
    qi = np.arange(SPAN)[:, None]
    ki = np.arange(2 * SPAN)[None, :]
    dist = np.maximum(qi + SPAN - ki, 0) * dilation
    max_exact = N_BUCKETS // 2
    d = np.maximum(dist, 1).astype(np.float64)
    large = max_exact + (np.log(d / max_exact) / math.log(MAX_DISTANCE / max_exact) * (N_BUCKETS - max_exact)).astype(np.int32)
    large = np.minimum(large, N_BUCKETS - 1)
    return np.where(dist < max_exact, dist, large).astype(np.int32)


def _bucket_tables():
    return jnp.asarray(np.stack([_bucket_table(r) for _, r in DILATED_PATTERNS]))


def _bias_table_call(rel_bias, buckets):
    def body(rb_ref, bk_ref, o_ref):
        for h in range(N_HEADS):
            bk = bk_ref[h // HEADS_PER_GROUP]

            def step(b, acc):
                return jnp.where(bk == b, rb_ref[b, h], acc)

            o_ref[h] = lax.fori_loop(0, N_BUCKETS, step, jnp.zeros((SPAN, 2 * SPAN), F32))

    return pl.pallas_call(
        body, name="rel_bias_table",
        in_specs=[pl.BlockSpec(memory_space=pltpu.SMEM), pl.BlockSpec(memory_space=pltpu.VMEM)],
        out_specs=pl.BlockSpec(memory_space=pltpu.VMEM),
        out_shape=jax.ShapeDtypeStruct((N_HEADS, SPAN, 2 * SPAN), F32),
    )(rel_bias, buckets)


def _bias_grad_call(dbias, buckets):
    def body(db_ref, bk_ref, o_ref, rows_ref):
        for h in range(N_HEADS):
            bk = bk_ref[h // HEADS_PER_GROUP]
            dv = db_ref[h]

            def step(b, carry):
                rows_ref[h, b] = jnp.sum(jnp.where(bk == b, dv, 0.0), axis=0, keepdims=True)
                return carry

            lax.fori_loop(0, N_BUCKETS, step, 0)
        o_ref[...] = jnp.sum(rows_ref[...], axis=-1, keepdims=True)

    out = pl.pallas_call(
        body, name="rel_bias_grad",
        in_specs=[pl.BlockSpec(memory_space=pltpu.VMEM), pl.BlockSpec(memory_space=pltpu.VMEM)],
        out_specs=pl.BlockSpec(memory_space=pltpu.VMEM),
        out_shape=jax.ShapeDtypeStruct((N_HEADS, N_BUCKETS, 1, 1), F32),
        scratch_shapes=[pltpu.VMEM((N_HEADS, N_BUCKETS, 1, 2 * SPAN), F32)],
    )(dbias, buckets)
    return out.reshape(N_HEADS, N_BUCKETS).T


def _dot_nt(a, b):
    return lax.dot_general(a, b, (((1,), (1,)), ((), ())), preferred_element_type=F32)


def _dot_nn(a, b):
    return lax.dot_general(a, b, (((1,), (0,)), ((), ())), preferred_element_type=F32)


def _dot_tn(a, b):
    return lax.dot_general(a, b, (((0,), (0,)), ((), ())), preferred_element_type=F32)


def _band_masks(n, nb):
    qi = lax.broadcasted_iota(jnp.int32, (SPAN, SPAN), 0)
    ki = lax.broadcasted_iota(jnp.int32, (SPAN, SPAN), 1)
    prev_ok = jnp.logical_and(ki >= qi, n > 0)
    cur_ok = ki <= qi
    next_ok = jnp.logical_and(ki >= qi, n < nb - 1)
    return prev_ok, cur_ok, next_ok


def _wide_band_mask(n):
    qi = lax.broadcasted_iota(jnp.int32, (SPAN, 2 * SPAN), 0)
    ki = lax.broadcasted_iota(jnp.int32, (SPAN, 2 * SPAN), 1)
    prev_ok = jnp.logical_and(jnp.logical_and(ki < SPAN, ki >= qi), n > 0)
    cur_ok = jnp.logical_and(ki >= SPAN, ki - SPAN <= qi)
    return jnp.logical_or(prev_ok, cur_ok)


def _attn_plan(S, group):
    r = DILATED_PATTERNS[group][1]
    hp, per = (HEADS_PER_GROUP, 1) if r == 1 else (2, 4)
    return r, S // (r * SPAN), hp, per


def _residue_rows(rho, r):
    return slice(None) if r == 1 else pl.ds(rho, SPAN, stride=r)


def _for_residues(r, per, fn):
    if r == per:
        for u in range(per):
            fn(u)
        return

    def step(i, carry):
        for u in range(per):
            fn(i * per + u)
        return carry

    lax.fori_loop(0, r // per, step, 0)


def _attn_fwd_call(proj, bias, group):
    S = proj.shape[0]
    r, nb, hp, per = _attn_plan(S, group)
    scale = HEAD_DIM ** -0.5
    kinds = ("q", "kp", "kc", "vp", "vc") if nb > 1 else ("q", "kc", "vc")

    per_kind = _refs_per_kind(r, hp)

    def body(*refs):
        ins = {kind: refs[i * per_kind:(i + 1) * per_kind] for i, kind in enumerate(kinds)}
        b_ref, o_ref, lse_ref = refs[len(kinds) * per_kind:]
        n = pl.program_id(1)
        prev_ok, cur_ok, _ = _band_masks(n, nb)

        band_ok = _wide_band_mask(n) if nb > 1 else cur_ok

        def residue(rho):
            rows = _residue_rows(rho, r)
            for j in range(hp):
                get = lambda kind: _head_rows(ins[kind], j, rows, r).astype(BF16)
                q = get("q")
                if nb > 1:
                    keys, vals, bias_j = jnp.concatenate([get("kp"), get("kc")], axis=0), jnp.concatenate([get("vp"), get("vc")], axis=0), b_ref[j]
                else:
                    keys, vals, bias_j = get("kc"), get("vc"), b_ref[j, :, SPAN:]
                s = jnp.where(band_ok, _dot_nt(q, keys) * scale + bias_j, NEG_INF)
                m = jnp.max(s, axis=-1, keepdims=True)
                p = jnp.exp(s - m)
                den = jnp.sum(p, axis=-1, keepdims=True)
                o_ref[j, rows, :] = _dot_nn(p.astype(BF16), vals) / den
                lse_ref[j, rows, :] = jnp.broadcast_to(m + jnp.log(den), (SPAN, HEAD_DIM))

        _for_residues(r, per, residue)

    in_specs = [_head_spec(r, nb, hp, kind, group, jj) for kind in kinds for jj in range(per_kind)]
    in_specs.append(pl.BlockSpec((hp, SPAN, 2 * SPAN), lambda j, n: (group * (HEADS_PER_GROUP // hp) + j, 0, 0)))
    out = pl.BlockSpec((hp, r * SPAN, HEAD_DIM), lambda j, n: (j, n, 0))
    return pl.pallas_call(
        body, name=f"attn_fwd_g{group}", grid=(HEADS_PER_GROUP // hp, nb),
        in_specs=in_specs, out_specs=[out] * 2,
        out_shape=[jax.ShapeDtypeStruct((HEADS_PER_GROUP, S, HEAD_DIM), F32)] * 2,
        compiler_params=_params(("parallel", "parallel"), VMEM_LIMIT),
    )(*([proj] * (len(in_specs) - 1)), bias)


_PROJ_PART = dict(q=0, qn=0, kp=1, kc=1, vp=2, vc=2)


def _refs_per_kind(r, hp):
    return 1 if r == 1 else hp


def _head_rows(refs, j, rows, r):
    return refs[0][:, j * HEAD_DIM:(j + 1) * HEAD_DIM] if r == 1 else refs[j][rows, :]


def _head_spec(r, nb, hp, kind, group, jj):
    if kind in _PROJ_PART:
        base = (_PROJ_PART[kind] * N_GROUPS + group) * HEADS_PER_GROUP
    else:
        base = 0
    if kind.endswith("p"):
        row = lambda n: jnp.maximum(n - 1, 0)
    elif kind.endswith("n"):
        row = lambda n: jnp.minimum(n + 1, nb - 1)
    else:
        row = lambda n: n
    if r == 1:
        return pl.BlockSpec((SPAN, hp * HEAD_DIM), lambda j, n: (row(n), base // hp + j))
    return pl.BlockSpec((r * SPAN, HEAD_DIM), lambda j, n: (row(n), base + j * hp + jj))


def _attn_merge_call(parts):
    S = parts[0].shape[1]

    def body(o1, s1, o2, s2, o3, s3, a_ref, ab_ref, lse_ref):
        for j in range(HEADS_PER_GROUP):
            sl = slice(j * HEAD_DIM, (j + 1) * HEAD_DIM)
            mx = jnp.maximum(jnp.maximum(s1[j], s2[j]), s3[j])
            w1 = jnp.exp(s1[j] - mx)
            w2 = jnp.exp(s2[j] - mx)
            w3 = jnp.exp(s3[j] - mx)
            den = w1 + w2 + w3
            a = (w1 * o1[j] + w2 * o2[j] + w3 * o3[j]) / den
            a_ref[:, sl] = a
            ab_ref[:, sl] = a.astype(BF16)
            lse_ref[:, sl] = mx + jnp.log(den)

    heads = pl.BlockSpec((HEADS_PER_GROUP, ROW_TILE, HEAD_DIM), lambda i: (0, i, 0))
    return pl.pallas_call(
        body, name="attn_merge", grid=(S // ROW_TILE,),
        in_specs=[heads] * 6, out_specs=[_row_spec(GROUP_WIDTH)] * 3,
        out_shape=[jax.ShapeDtypeStruct((S, GROUP_WIDTH), F32), jax.ShapeDtypeStruct((S, GROUP_WIDTH), BF16),
                   jax.ShapeDtypeStruct((S, GROUP_WIDTH), F32)],
        compiler_params=_params(("parallel",)),
    )(*parts)


def _attn_delta_call(a, da):
    S = a.shape[0]

    def body(a_ref, da_ref, d_ref):
        for j in range(HEADS_PER_GROUP):
            sl = slice(j * HEAD_DIM, (j + 1) * HEAD_DIM)
            d = jnp.sum(a_ref[:, sl] * da_ref[:, sl], axis=-1, keepdims=True)
            d_ref[:, sl] = jnp.broadcast_to(d, (ROW_TILE, HEAD_DIM))

    return pl.pallas_call(
        body, name="attn_delta", grid=(S // ROW_TILE,),
        in_specs=[_row_spec(GROUP_WIDTH)] * 2, out_specs=_row_spec(GROUP_WIDTH),
        out_shape=jax.ShapeDtypeStruct((S, GROUP_WIDTH), F32),
        compiler_params=_params(("parallel",)),
    )(a, da)


def _attn_bwd_call(proj, bias, da, lse, delta, group):
    S = proj.shape[0]
    r, nb, hp, per = _attn_plan(S, group)
    scale = HEAD_DIM ** -0.5
    kinds = ("q", "qn", "kp", "kc", "vp", "vc", "da", "dan", "lse", "lsen", "dl", "dln") if nb > 1 else ("q", "kc", "vc", "da", "lse", "dl")
    source = dict(da=da, dan=da, lse=lse, lsen=lse, dl=delta, dln=delta)

    per_kind = _refs_per_kind(r, hp)

    def body(*refs):
        ins = {kind: refs[i * per_kind:(i + 1) * per_kind] for i, kind in enumerate(kinds)}
        b_ref, dq_ref, dk_ref, dv_ref, db_ref = refs[len(kinds) * per_kind:]
        n = pl.program_id(1)
        prev_ok, cur_ok, next_ok = _band_masks(n, nb)

        @pl.when(n == 0)
        def _():
            db_ref[...] = jnp.zeros_like(db_ref)

        band_ok = _wide_band_mask(n) if nb > 1 else cur_ok

        def residue(rho):
            rows = _residue_rows(rho, r)
            for j in range(hp):
                get = lambda kind: _head_rows(ins[kind], j, rows, r)
                q = get("q").astype(BF16)
                kc = get("kc").astype(BF16)
                vc = get("vc").astype(BF16)
                dav = get("da").astype(BF16)
                lse_q, dl_q = get("lse"), get("dl")
                if nb == 1:
                    pc = jnp.exp(jnp.where(cur_ok, _dot_nt(q, kc) * scale + b_ref[j, :, SPAN:], NEG_INF) - lse_q)
                    dsc = pc * (_dot_nt(dav, vc) - dl_q)
                    dsc_b = dsc.astype(BF16)
                    dq = _dot_nn(dsc_b, kc)
                    dk = _dot_tn(dsc_b, q)
                    dv = _dot_tn(pc.astype(BF16), dav)
                    db_ref[j, :, SPAN:] += dsc
                else:
                    qn = get("qn").astype(BF16)
                    dan = get("dan").astype(BF16)
                    keys = jnp.concatenate([get("kp").astype(BF16), kc], axis=0)
                    vals = jnp.concatenate([get("vp").astype(BF16), vc], axis=0)
                    wide = lambda t: jnp.concatenate([t, t], axis=1)
                    p = jnp.exp(jnp.where(band_ok, _dot_nt(q, keys) * scale + b_ref[j], NEG_INF) - wide(lse_q))
                    ds = p * (_dot_nt(dav, vals) - wide(dl_q))
                    dq = _dot_nn(ds.astype(BF16), keys)
                    db_ref[j] += ds
                    pn = jnp.exp(jnp.where(next_ok, _dot_nt(qn, kc) * scale + b_ref[j, :, :SPAN], NEG_INF) - get("lsen"))
                    dsn = pn * (_dot_nt(dan, vc) - get("dln"))
                    both = lambda cur_part, next_part: jnp.concatenate([cur_part.astype(BF16), next_part.astype(BF16)], axis=0)
                    dk = _dot_tn(both(ds[:, SPAN:], dsn), jnp.concatenate([q, qn], axis=0))
                    dv = _dot_tn(both(p[:, SPAN:], pn), jnp.concatenate([dav, dan], axis=0))
                dq_ref[j, rows, :] = dq * scale
                dk_ref[j, rows, :] = dk * scale
                dv_ref[j, rows, :] = dv

        _for_residues(r, per, residue)

    per_group = HEADS_PER_GROUP // hp
    band = (hp, SPAN, 2 * SPAN)
    in_specs = [_head_spec(r, nb, hp, kind, group, jj) for kind in kinds for jj in range(per_kind)]
    in_specs.append(pl.BlockSpec(band, lambda j, n: (group * per_group + j, 0, 0)))
    operands = [source.get(kind, proj) for kind in kinds for _ in range(per_kind)] + [bias]
    out = pl.BlockSpec((hp, r * SPAN, HEAD_DIM), lambda j, n: (j, n, 0))
    return pl.pallas_call(
        body, name=f"attn_bwd_g{group}", grid=(per_group, nb),
        in_specs=in_specs,
        out_specs=[out] * 3 + [pl.BlockSpec(band, lambda j, n: (j, 0, 0))],
        out_shape=[jax.ShapeDtypeStruct((HEADS_PER_GROUP, S, HEAD_DIM), F32)] * 3
        + [jax.ShapeDtypeStruct((HEADS_PER_GROUP, SPAN, 2 * SPAN), F32)],
        compiler_params=_params(("parallel", "arbitrary"), VMEM_LIMIT),
    )(*operands)


def _dproj_call(dqkv, tails):
    S = tails[0].shape[0]
    width = len(dqkv) * GROUP_WIDTH + sum(t.shape[1] for t in tails)

    def body(*refs):
        o_ref = refs[-1]
        col = 0
        for ref in refs[:len(dqkv)]:
            for j in range(HEADS_PER_GROUP):
                o_ref[:, col:col + HEAD_DIM] = ref[j].astype(BF16)
                col += HEAD_DIM
        for ref in refs[len(dqkv):-1]:
            o_ref[:, col:col + ref.shape[1]] = ref[...]
            col += ref.shape[1]

    heads = pl.BlockSpec((HEADS_PER_GROUP, ROW_TILE, HEAD_DIM), lambda i: (0, i, 0))
    return pl.pallas_call(
        body, name="dproj_assemble", grid=(S // ROW_TILE,),
        in_specs=[heads] * len(dqkv) + [_row_spec(t.shape[1]) for t in tails],
        out_specs=_row_spec(width), out_shape=jax.ShapeDtypeStruct((S, width), BF16),
        compiler_params=_params(("parallel",)),
    )(*dqkv, *tails)


def _tap_rows(xpad_ref, t0, k, width, pad):
    return xpad_ref[pl.ds(t0 + (pad - (width - 1 - k)), TIME_BLOCK), :]


def _conv_block(xpad_ref, t0, w_ref, width, pad):
    acc = None
    for k in range(width):
        term = w_ref[k:k + 1, :] * _tap_rows(xpad_ref, t0, k, width, pad)
        acc = term if acc is None else acc + term
    return acc


def _conv_transpose_block(dpad_ref, t0, w_ref, width):
    acc = None
    for k in range(width):
        term = w_ref[k:k + 1, :] * dpad_ref[pl.ds(t0 + (width - 1 - k), TIME_BLOCK), :]
        acc = term if acc is None else acc + term
    return acc


def _conv_weight_grad(xpad_ref, t0, dy, dw_ref, width, pad):
    for k in range(width):
        dw_ref[k:k + 1, :] += jnp.sum(dy * _tap_rows(xpad_ref, t0, k, width, pad), axis=0, keepdims=True)


def _time_loop(S, step, skip_first=0, skip_last=0):
    def it(tb, carry):
        step(pl.multiple_of(tb * TIME_BLOCK, TIME_BLOCK))
        return carry

    lax.fori_loop(skip_first, S // TIME_BLOCK - skip_last, it, 0)


def _fill_head(head_ref, x_ref, pad):
    head_ref[0:pad, :] = jnp.zeros((pad, LANES), F32)
    head_ref[pad:, :] = x_ref[0:TIME_BLOCK, :]


def _fill_tail(tail_ref, x_ref, pad):
    S = x_ref.shape[0]
    tail_ref[0:TIME_BLOCK, :] = x_ref[S - TIME_BLOCK:S, :]
    tail_ref[TIME_BLOCK:, :] = jnp.zeros((pad, LANES), F32)


def _conv_fwd_call(proj, col0, w, b):
    S = proj.shape[0]
    C = w.shape[1]
    nt = C // LANES
    v0, g0 = col0 // LANES, (col0 + C) // LANES

    def body(val_ref, gate_ref, w_ref, b_ref, o_ref, pad_ref):
        pad_ref[0:CONV_PAD, :] = jnp.zeros((CONV_PAD, LANES), F32)
        pad_ref[CONV_PAD:, :] = val_ref[...] * _sigmoid(gate_ref[...])

        def step(t0):
            o_ref[pl.ds(t0, TIME_BLOCK), :] = _conv_block(pad_ref, t0, w_ref, CONV_WIDTH, CONV_PAD) + b_ref[...]

        _time_loop(S, step)

    seq = lambda off: pl.BlockSpec((S, LANES), lambda i: (0, off + i))
    return pl.pallas_call(
        body, name="conv_module", grid=(nt,),
        in_specs=[seq(v0), seq(g0), pl.BlockSpec((CONV_WIDTH, LANES), lambda i: (0, i)), pl.BlockSpec((1, LANES), lambda i: (0, i))],
        out_specs=seq(0), out_shape=jax.ShapeDtypeStruct((S, C), F32),
        scratch_shapes=[pltpu.VMEM((S + CONV_PAD, LANES), F32)],
        compiler_params=_params(("parallel",)),
    )(proj, proj, w, b)


def _conv_bwd_call(proj, col0, w, dc1):
    S = proj.shape[0]
    C = w.shape[1]
    nt = C // LANES
    v0, g0 = col0 // LANES, (col0 + C) // LANES

    def body(val_ref, gate_ref, w_ref, dy_ref, dval_ref, dgate_ref, dw_ref, db_ref, xpad_ref, tail_ref, dwacc_ref):
        xpad_ref[0:CONV_PAD, :] = jnp.zeros((CONV_PAD, LANES), F32)
        xpad_ref[CONV_PAD:, :] = val_ref[...] * _sigmoid(gate_ref[...])
        _fill_tail(tail_ref, dy_ref, CONV_PAD)
        dwacc_ref[...] = jnp.zeros_like(dwacc_ref)

        def block(t0, dy_src, dy_t0):
            rows = pl.ds(t0, TIME_BLOCK)
            _conv_weight_grad(xpad_ref, t0, dy_ref[rows, :], dwacc_ref, CONV_WIDTH, CONV_PAD)
            dc0 = _conv_transpose_block(dy_src, dy_t0, w_ref, CONV_WIDTH)
            sg = _sigmoid(gate_ref[rows, :])
            dval_ref[rows, :] = (dc0 * sg).astype(BF16)
            dgate_ref[rows, :] = (dc0 * val_ref[rows, :] * sg * (1.0 - sg)).astype(BF16)

        _time_loop(S, lambda t0: block(t0, dy_ref, t0), skip_last=1)
        block(S - TIME_BLOCK, tail_ref, 0)
        dw_ref[...] = dwacc_ref[...]
        db_ref[...] = jnp.sum(dy_ref[...], axis=0, keepdims=True)

    seq = lambda off: pl.BlockSpec((S, LANES), lambda i: (0, off + i))
    return pl.pallas_call(
        body, name="conv_module_bwd", grid=(nt,),
        in_specs=[seq(v0), seq(g0), pl.BlockSpec((CONV_WIDTH, LANES), lambda i: (0, i)), seq(0)],
        out_specs=[seq(0), seq(0), pl.BlockSpec((CONV_PAD, LANES), lambda i: (0, i)), pl.BlockSpec((1, LANES), lambda i: (0, i))],
        out_shape=[jax.ShapeDtypeStruct((S, C), BF16), jax.ShapeDtypeStruct((S, C), BF16),
                   jax.ShapeDtypeStruct((CONV_PAD, C), F32), jax.ShapeDtypeStruct((1, C), F32)],
        scratch_shapes=[pltpu.VMEM((S + CONV_PAD, LANES), F32), pltpu.VMEM((TIME_BLOCK + CONV_PAD, LANES), F32),
                        pltpu.VMEM((CONV_PAD, LANES), F32)],
        compiler_params=_params(("parallel",)),
    )(proj, proj, w, dc1)


def _ffn_fwd_call(u, w, b):
    S, C2 = u.shape
    C = C2 // 2
    nt = C // LANES

    def body(ug_ref, uv_ref, wg_ref, wv_ref, bg_ref, bv_ref, f_ref, hg_ref, hv_ref):
        _fill_head(hg_ref, ug_ref, FFN_PAD)
        _fill_head(hv_ref, uv_ref, FFN_PAD)

        def block(t0, xg_ref, xv_ref, x_t0, pad):
            cg = _conv_block(xg_ref, x_t0, wg_ref, FFN_CONV_WIDTH, pad) + bg_ref[...]
            cv = _conv_block(xv_ref, x_t0, wv_ref, FFN_CONV_WIDTH, pad) + bv_ref[...]
            f_ref[pl.ds(t0, TIME_BLOCK), :] = (_gelu(cg) * cv).astype(BF16)

        block(0, hg_ref, hv_ref, 0, FFN_PAD)
        _time_loop(S, lambda t0: block(t0, ug_ref, uv_ref, t0, 0), skip_first=1)

    seq = lambda off: pl.BlockSpec((S, LANES), lambda i: (0, off + i))
    wsp = lambda off: pl.BlockSpec((FFN_CONV_WIDTH, LANES), lambda i: (0, off + i))
    bsp = lambda off: pl.BlockSpec((1, LANES), lambda i: (0, off + i))
    return pl.pallas_call(
        body, name="ffn_conv_geglu", grid=(nt,),
        in_specs=[seq(0), seq(nt), wsp(0), wsp(nt), bsp(0), bsp(nt)],
        out_specs=seq(0), out_shape=jax.ShapeDtypeStruct((S, C), BF16),
        scratch_shapes=[pltpu.VMEM((FFN_PAD + TIME_BLOCK, LANES), F32)] * 2,
        compiler_params=_params(("parallel",)),
    )(u, u, w, w, b, b)


def _ffn_bwd_call(u, w, b, df):
    S, C2 = u.shape
    C = C2 // 2
    nt = C // LANES

    def body(ug_ref, uv_ref, wg_ref, wv_ref, bg_ref, bv_ref, df_ref,
             du_ref, dwg_ref, dwv_ref, dbg_ref, dbv_ref,
             hg_ref, hv_ref, dg_ref, dv_ref, dwg_acc, dwv_acc, dbg_acc, dbv_acc):
        zeros = jnp.zeros((FFN_PAD, LANES), F32)
        _fill_head(hg_ref, ug_ref, FFN_PAD)
        _fill_head(hv_ref, uv_ref, FFN_PAD)
        dg_ref[S:, :] = zeros
        dv_ref[S:, :] = zeros
        dwg_acc[...] = jnp.zeros_like(dwg_acc)
        dwv_acc[...] = jnp.zeros_like(dwv_acc)
        dbg_acc[...] = jnp.zeros_like(dbg_acc)
        dbv_acc[...] = jnp.zeros_like(dbv_acc)

        def first(t0, xg_ref, xv_ref, x_t0, pad):
            rows = pl.ds(t0, TIME_BLOCK)
            cg = _conv_block(xg_ref, x_t0, wg_ref, FFN_CONV_WIDTH, pad) + bg_ref[...]
            cv = _conv_block(xv_ref, x_t0, wv_ref, FFN_CONV_WIDTH, pad) + bv_ref[...]
            dfb = df_ref[rows, :]
            gelu, gelu_grad = _gelu_and_grad(cg)
            dcg = dfb * cv * gelu_grad
            dcv = dfb * gelu
            dg_ref[rows, :] = dcg
            dv_ref[rows, :] = dcv
            _conv_weight_grad(xg_ref, x_t0, dcg, dwg_acc, FFN_CONV_WIDTH, pad)
            _conv_weight_grad(xv_ref, x_t0, dcv, dwv_acc, FFN_CONV_WIDTH, pad)
            dbg_acc[...] += jnp.sum(dcg, axis=0, keepdims=True)
            dbv_acc[...] += jnp.sum(dcv, axis=0, keepdims=True)

        def second(t0):
            rows = pl.ds(t0, TIME_BLOCK)
            du_ref[0, rows, :] = _conv_transpose_block(dg_ref, t0, wg_ref, FFN_CONV_WIDTH).astype(BF16)
            du_ref[1, rows, :] = _conv_transpose_block(dv_ref, t0, wv_ref, FFN_CONV_WIDTH).astype(BF16)

        first(0, hg_ref, hv_ref, 0, FFN_PAD)
        _time_loop(S, lambda t0: first(t0, ug_ref, uv_ref, t0, 0), skip_first=1)
        _time_loop(S, second)
        dwg_ref[...] = dwg_acc[...]
        dwv_ref[...] = dwv_acc[...]
        dbg_ref[...] = dbg_acc[...]
        dbv_ref[...] = dbv_acc[...]

    seq = lambda off: pl.BlockSpec((S, LANES), lambda i: (0, off + i))
    wsp = lambda off: pl.BlockSpec((FFN_CONV_WIDTH, LANES), lambda i: (0, off + i))
    bsp = lambda off: pl.BlockSpec((1, LANES), lambda i: (0, off + i))
    return pl.pallas_call(
        body, name="ffn_conv_geglu_bwd", grid=(nt,),
        in_specs=[seq(0), seq(nt), wsp(0), wsp(nt), bsp(0), bsp(nt), seq(0)],
        out_specs=[pl.BlockSpec((2, S, LANES), lambda i: (0, 0, i)),
                   pl.BlockSpec((SUBLANES, LANES), lambda i: (0, i)), pl.BlockSpec((SUBLANES, LANES), lambda i: (0, i)),
                   bsp(0), bsp(0)],
        out_shape=[jax.ShapeDtypeStruct((2, S, C), BF16)] + [jax.ShapeDtypeStruct((SUBLANES, C), F32)] * 2
        + [jax.ShapeDtypeStruct((1, C), F32)] * 2,
        scratch_shapes=[pltpu.VMEM((FFN_PAD + TIME_BLOCK, LANES), F32)] * 2 + [pltpu.VMEM((S + FFN_PAD, LANES), F32)] * 2
        + [pltpu.VMEM((SUBLANES, LANES), F32)] * 2
        + [pltpu.VMEM((1, LANES), F32)] * 2,
        compiler_params=_params(("parallel",)),
    )(u, u, w, w, b, b, df)


def _adamw(w_ref, g_ref, m_ref, v_ref, d_ref, mo_ref, vo_ref):
    gv = g_ref[...]
    mn = ADAM_B1 * m_ref[...] + (1.0 - ADAM_B1) * gv
    vn = ADAM_B2 * v_ref[...] + (1.0 - ADAM_B2) * (gv * gv)
    mo_ref[...] = mn
    vo_ref[...] = vn
    m_hat = mn * (1.0 / (1.0 - ADAM_B1 ** ADAM_STEP))
    v_hat = vn * (1.0 / (1.0 - ADAM_B2 ** ADAM_STEP))
    d_ref[...] = -ADAM_LR * (m_hat / (jnp.sqrt(v_hat) + ADAM_EPS) + ADAM_WD * w_ref[...])


def _adamw_call(w, g, m, v, name):
    R, C = w.shape
    tr = _row_tile(R, C)

    def body(w_ref, g_ref, m_ref, v_ref, go_ref, d_ref, mo_ref, vo_ref):
        go_ref[...] = g_ref[...]
        _adamw(w_ref, g_ref, m_ref, v_ref, d_ref, mo_ref, vo_ref)

    spec = pl.BlockSpec((tr, C), lambda i: (i, 0))
    return pl.pallas_call(
        body, name=name, grid=(R // tr,),
        in_specs=[spec] * 4, out_specs=[spec] * 4,
        out_shape=[jax.ShapeDtypeStruct((R, C), F32)] * 4,
        compiler_params=_params(("parallel",)),
    )(w, g, m, v)


def _adamw_small_call(ws, gs, ms, vs):
    n = len(ws)

    def body(*refs):
        w_refs, g_refs, m_refs, v_refs, d_refs, mo_refs, vo_refs = (refs[i * n:(i + 1) * n] for i in range(7))
        for i in range(n):
            _adamw(w_refs[i], g_refs[i], m_refs[i], v_refs[i], d_refs[i], mo_refs[i], vo_refs[i])

    whole = pl.BlockSpec(memory_space=pltpu.VMEM)
    outs = pl.pallas_call(
        body, name="adamw_small",
        in_specs=[whole] * (4 * n), out_specs=[whole] * (3 * n),
        out_shape=[jax.ShapeDtypeStruct(w.shape, F32) for w in ws] * 3,
    )(*ws, *gs, *ms, *vs)
    return outs[:n], outs[n:2 * n], outs[2 * n:]


def _position():
    return lax.axis_index("x"), lax.axis_index("y"), lax.axis_index("c")


def _chip_peers(x, y):
    return [(x, 1 - y), (1 - x, y), (1 - x, 1 - y)]


def _half_rows(ref, core, rows):
    h = rows // 2
    start = pl.multiple_of(core * h, 16)
    return ref.at[pl.ds(start, h), :] if len(ref.shape) == 2 else ref.at[:, pl.ds(start, h), :]


def _shard_half(ref, shard, core, rows):
    h = rows // 2
    return ref.at[shard, pl.ds(pl.multiple_of(core * h, 16), h), :]


ANY = pl.BlockSpec(memory_space=pl.ANY)


def _first_hop_copies(srcs, lands):
    x, y, c = _position()
    chip = 2 * x + y
    targets = [(px, py, c) for px, py in _chip_peers(x, y)] + [(x, y, 1 - c)]
    rows = srcs[0].shape[0]
    out = []
    for i, (s, l) in enumerate(zip(srcs, lands)):
        for k, dev in enumerate(targets):
            if i == 0 and k < 3:
                out.append((_half_rows(s, c, rows), _shard_half(l, chip, c, rows), dev, k))
            else:
                out.append((s, l.at[chip], dev, len(targets) * i + k))
    return out


def _second_hop_copies(srcs, lands):
    x, y, c = _position()
    rows = lands[0].shape[1]
    out = []
    for k, (px, py) in enumerate(_chip_peers(x, y)):
        half = _shard_half(lands[0], 2 * px + py, c, rows)
        out.append((half, half, (x, y, 1 - c), k))
    return out


HBM_SPEC = pl.BlockSpec(memory_space=pltpu.HBM)
SEM_SPEC = pl.BlockSpec(memory_space=pltpu.SEMAPHORE)
DATAFLOW = pltpu.SideEffectType.DATAFLOW_SIDE_EFFECTING


def _in_hbm(a):
    return pltpu.with_memory_space_constraint(a, pltpu.HBM)


def _split_start(name, groups, after, carry=None):
    spans, arrays = [], []
    for srcs, lands, _, _ in groups:
        spans.append((len(arrays), len(srcs), len(lands)))
        arrays += list(srcs) + list(lands)
    if carry is not None:
        arrays.append(carry)
    na, ng = len(arrays), len(groups)

    def body(*refs):
        sems, token = refs[na + 1:na + 1 + 2 * ng], refs[-1]
        for g, (_, _, _, copies) in enumerate(groups):
            off, ns, nl = spans[g]
            for src, dst, dev, idx in copies(refs[off:off + ns], refs[off + ns:off + ns + nl]):
                pltpu.make_async_remote_copy(src_ref=src, dst_ref=dst, send_sem=sems[2 * g].at[idx], recv_sem=sems[2 * g + 1].at[idx],
                                             device_id=dev, device_id_type=MESH).start()
        token[...] = jnp.zeros_like(token)

    outs = pl.pallas_call(
        body, name=name,
        in_specs=[HBM_SPEC] * na + [ANY],
        out_specs=[SEM_SPEC] * (2 * ng) + [HBM_SPEC] * na + [pl.BlockSpec(memory_space=pltpu.VMEM)],
        out_shape=[pltpu.SemaphoreType.DMA((n_sems,)) for _, _, n_sems, _ in groups for _ in range(2)]
        + [pltpu.HBM(a.shape, a.dtype) for a in arrays] + [jax.ShapeDtypeStruct((SUBLANES, LANES), F32)],
        input_output_aliases={i: 2 * ng + i for i in range(na)},
        compiler_params=pltpu.CompilerParams(has_side_effects=DATAFLOW),
    )(*[_in_hbm(a) for a in arrays], after)
    started = []
    for g, (off, ns, nl) in enumerate(spans):
        thru = outs[2 * ng + off:2 * ng + off + ns + nl]
        started.append(dict(send=outs[2 * g], recv=outs[2 * g + 1], srcs=list(thru[:ns]), lands=list(thru[ns:]),
                            tile=outs[-1], token=outs[-1][0, 0], carry=None if carry is None else outs[2 * ng + na - 1]))
    return started


def _split_wait(name, started, copies, after):
    n, m = len(started["srcs"]), len(started["lands"])

    def body(*refs):
        src_refs, land_refs = refs[:n], refs[n:n + m]
        send_sem, recv_sem = refs[n + m], refs[n + m + 1]
        for src, dst, dev, idx in copies(src_refs, land_refs):
            cp = pltpu.make_async_remote_copy(src_ref=src, dst_ref=dst, send_sem=send_sem.at[idx], recv_sem=recv_sem.at[idx],
                                              device_id=dev, device_id_type=MESH)
            cp.wait_send()
            cp.wait_recv()

    arrays = started["srcs"] + started["lands"]
    outs = pl.pallas_call(
        body, name=name,
        in_specs=[HBM_SPEC] * (n + m) + [SEM_SPEC, SEM_SPEC, ANY],
        out_specs=[HBM_SPEC] * (n + m),
        out_shape=[pltpu.HBM(a.shape, a.dtype) for a in arrays],
        input_output_aliases={i: i for i in range(n + m)},
        compiler_params=pltpu.CompilerParams(has_side_effects=DATAFLOW),
    )(*arrays, started["send"], started["recv"], after)
    return list(outs)


def _gather_copies(srcs, lands):
    x, y, c = _position()
    chip = 2 * x + y
    targets = [(px, py, c) for px, py in _chip_peers(x, y)] + [(x, y, 1 - c)]
    return [(s, l.at[chip], dev, len(targets) * i + k) for i, (s, l) in enumerate(zip(srcs, lands)) for k, dev in enumerate(targets)]


def _sibling_copies(srcs, lands):
    x, y, c = _position()
    return [(_half_rows(srcs[0], 1 - c, srcs[0].shape[1]), lands[0], (x, y, 1 - c), 0)]


def _exchange_copies(srcs, lands):
    x, y, c = _position()
    return [(srcs[0].at[2 * px + py], lands[0].at[k], (px, py, c), k) for k, (px, py) in enumerate(_chip_peers(x, y))]


def _pair_sum_call(grad, recv, core, name):
    _, h, B = recv.shape
    tr = _row_tile(h, B)

    def body(core_ref, g_ref, r_ref, o_ref, ob_ref):
        s = g_ref[...] + r_ref[...]
        o_ref[...] = s
        ob_ref[...] = s.astype(BF16)

    g_spec = pl.BlockSpec((None, tr, B), lambda q, i, core_ref: (q, core_ref[0] * (h // tr) + i, 0))
    spec = pl.BlockSpec((None, tr, B), lambda q, i, core_ref: (q, i, 0))
    return pl.pallas_call(
        body, name=name,
        grid_spec=pltpu.PrefetchScalarGridSpec(num_scalar_prefetch=1, grid=(N_CHIPS, h // tr), in_specs=[g_spec, spec],
                                               out_specs=[spec, spec]),
        out_shape=[jax.ShapeDtypeStruct(recv.shape, F32), jax.ShapeDtypeStruct(recv.shape, BF16)],
        compiler_params=_params(("parallel", "parallel")),
    )(core, grad, recv)


def _chip_sum_call(partial, recv, chip_core, name):
    _, h, B = recv.shape
    tr = _row_tile(h, B)

    def body(cc_ref, p_ref, r_ref, o_ref):
        o_ref[...] = ((p_ref[...] + r_ref[0].astype(F32)) + r_ref[1].astype(F32)) + r_ref[2].astype(F32)

    return pl.pallas_call(
        body, name=name,
        grid_spec=pltpu.PrefetchScalarGridSpec(
            num_scalar_prefetch=1, grid=(h // tr,),
            in_specs=[pl.BlockSpec((None, tr, B), lambda i, cc_ref: (cc_ref[0], i, 0)),
                      pl.BlockSpec((3, tr, B), lambda i, cc_ref: (0, i, 0))],
            out_specs=pl.BlockSpec((tr, B), lambda i, cc_ref: (cc_ref[1] * (h // tr) + i, 0))),
        out_shape=jax.ShapeDtypeStruct((2 * h, B), F32),
        compiler_params=_params(("parallel",)),
    )(chip_core, partial, recv)


def _sibling_assemble_call(shards, name="grad_sibling_assemble"):
    n = len(shards)

    def body(*refs):
        ins, outs = refs[:n], refs[n:2 * n]
        send_sems, recv_sems = refs[2 * n:]
        x, y, c = _position()
        copies = []
        for i in range(n):
            rows = shards[i].shape[0]
            cp = pltpu.make_async_remote_copy(src_ref=_half_rows(ins[i], c, rows), dst_ref=_half_rows(outs[i], c, rows),
                                              send_sem=send_sems.at[i], recv_sem=recv_sems.at[i],
                                              device_id=(x, y, 1 - c), device_id_type=MESH)
            cp.start()
            copies.append(cp)
        for cp in copies:
            cp.wait()

    return pl.pallas_call(
        body, name=name,
        in_specs=[ANY] * n, out_specs=[ANY] * n,
        out_shape=[jax.ShapeDtypeStruct(s.shape, F32) for s in shards],
        input_output_aliases={i: i for i in range(n)},
        scratch_shapes=[pltpu.SemaphoreType.DMA((n,)), pltpu.SemaphoreType.DMA((n,))],
    )(*shards)


N_DEVICES = 8


def _allsum_copies(srcs, lands):
    x, y, c = _position()
    me = 4 * x + 2 * y + c
    out = []
    for k in range(1, N_DEVICES):
        peer = (1 - x if k & 4 else x, 1 - y if k & 2 else y, 1 - c if k & 1 else c)
        out.append((srcs[0], lands[0].at[me], peer, k - 1))
    return out


def _ordered_sum_call(mine, landed, me_chip, shapes, sharded_cols):
    rows = mine.shape[0]
    outs = [(s[0], n) if n else s for s, n in zip(shapes, sharded_cols)]

    def body(mc_ref, x_ref, l_ref, *refs):
        acc_ref = refs[-1]
        acc = jnp.where(mc_ref[0] == 0, x_ref[...], l_ref[0])
        for d in range(1, N_DEVICES):
            acc = acc + jnp.where(mc_ref[0] == d, x_ref[...], l_ref[d])
        acc_ref[...] = acc
        first = 0
        for o_ref, (r, c), n in zip(refs[:-1], shapes, sharded_cols):
            per_row = c // LANES

            def unpack(chip, o_ref=o_ref, r=r, n=n, per_row=per_row, first=first):
                for i in range(r):
                    for j in range((n or per_row * LANES) // LANES):
                        src = first + i * per_row + chip * ((n or 0) // LANES) + j
                        o_ref[i:i + 1, j * LANES:(j + 1) * LANES] = acc_ref[src:src + 1, :]

            if n:
                for q in range(N_CHIPS):
                    pl.when(mc_ref[1] == q)(functools.partial(unpack, q))
            else:
                unpack(0)
            first += r * per_row

    results = pl.pallas_call(
        body, name="small_grad_sum",
        in_specs=[pl.BlockSpec(memory_space=pltpu.SMEM), pl.BlockSpec(memory_space=pltpu.VMEM), pl.BlockSpec(memory_space=pltpu.VMEM)],
        out_specs=[pl.BlockSpec(memory_space=pltpu.VMEM)] * len(outs),
        out_shape=[jax.ShapeDtypeStruct(s, F32) for s in outs],
        scratch_shapes=[pltpu.VMEM((rows, LANES), F32)],
    )(me_chip, mine, landed)
    return results


def _pack(arrays):
    flat = jnp.concatenate([a.reshape(-1).astype(F32) for a in arrays])
    rows = -(-flat.shape[0] // LANES)
    rows = -(-rows // SUBLANES) * SUBLANES
    flat = jnp.pad(flat, (0, rows * LANES - flat.shape[0]))
    return flat.reshape(rows, LANES)


def _local_step(xs, target, P, late_weights, on_grad):
    S, D = xs.shape
    qkv_width = 3 * N_HEADS * HEAD_DIM
    glu_col0, gate_col0 = qkv_width, qkv_width + 2 * D
    shard_major = lambda g: g.reshape(N_CHIPS, g.shape[0] // N_CHIPS, g.shape[1])

    h1 = _rms_fwd_call(xs, P["norm_mix_pre"])
    buckets = _bucket_tables()
    bias = _bias_table_call(P["rel_bias"] + 0.0 * h1[0, 0].astype(F32), buckets)
    P = dict(P, **late_weights("in", bias))
    proj = _matmul(h1, P["w_in"], "nn", "proj_in")
    parts = []
    for g in range(N_GROUPS):
        parts += _attn_fwd_call(proj, bias, g)
    a, a_bf, lse = _attn_merge_call(parts)
    P = dict(P, **late_weights("mix", a_bf))
    y_a = _matmul(a_bf, P["w_attn_out"], "nn", "attn_out")
    c1 = _conv_fwd_call(proj, glu_col0, P["conv_dw_w"], P["conv_dw_b"])
    cact = _ln_silu_call(c1, P["conv_ln_g"], P["conv_ln_b"])
    y_c = _matmul(cact, P["conv_pw_w"], "nn", "conv_pw")
    mixed = _mix_call(proj, gate_col0, P["b_gate"], y_a, y_c)
    out = _matmul(mixed, P["w_out"], "nn", "mix_out")
    x1, h2 = _res1_call(xs, out, P["norm_mix_post"], P["norm_ffn_pre"])
    P = dict(P, **late_weights("up", h2))
    u = _matmul(h2, P["w_up"], "nn", "ffn_up")
    f = _ffn_fwd_call(u, P["ffn_conv_w"], P["ffn_conv_b"])
    P = dict(P, **late_weights("down", f))
    yff = _matmul(f, P["w_down"], "nn", "ffn_down")
    loss_tile, dx2, dyff, dg_ffn_post = _loss_call(yff, x1, P["norm_ffn_post"], target)

    G = {}
    G["norm_ffn_post"] = dg_ffn_post
    on_grad("w_down", shard_major(_matmul(f, dyff, "tn", "ffn_down_dw")))
    df = _matmul(dyff, P["w_down"], "nt", "ffn_down_dx")
    du, dwg, dwv, dbg, dbv = _ffn_bwd_call(u, P["ffn_conv_w"], P["ffn_conv_b"], df)
    G["ffn_conv_w"] = jnp.concatenate([dwg[:FFN_CONV_WIDTH], dwv[:FFN_CONV_WIDTH]], axis=1)
    G["ffn_conv_b"] = jnp.concatenate([dbg, dbv], axis=1)
    du = on_grad("w_up", _matmul(h2, du, "tn", "ffn_up_dw", out_shards=True), carry=du)
    dh2 = _matmul(du, P["w_up"], "nt", "ffn_up_dx")
    dx1, dout, G["norm_ffn_pre"], G["norm_mix_post"] = _mid_bwd_call(x1, P["norm_ffn_pre"], dh2, dx2, out, P["norm_mix_post"])
    on_grad("w_out", shard_major(_matmul(mixed, dout, "tn", "mix_out_dw")))
    dmixed = _matmul(dout, P["w_out"], "nt", "mix_out_dx")
    dya, dyc, dga, dgc, dba, dbc = _mix_bwd_call(dmixed, proj, gate_col0, P["b_gate"], y_a, y_c)
    G["b_gate"] = jnp.concatenate([dba, dbc], axis=1)
    on_grad("w_attn_out", _matmul(a_bf, dya, "tn", "attn_out_dw", out_shards=True))
    dyc = on_grad("conv_pw_w", shard_major(_matmul(cact, dyc, "tn", "conv_pw_dw")), carry=dyc)
    da = _matmul(dya, P["w_attn_out"], "nt", "attn_out_dx")
    dcact = _matmul(dyc, P["conv_pw_w"], "nt", "conv_pw_dx")
    dc1, G["conv_ln_g"], G["conv_ln_b"] = _ln_silu_bwd_call(c1, P["conv_ln_g"], P["conv_ln_b"], dcact)
    dval, dgate, dw_dw, G["conv_dw_b"] = _conv_bwd_call(proj, glu_col0, P["conv_dw_w"], dc1)
    G["conv_dw_w"] = dw_dw[:CONV_WIDTH]
    delta = _attn_delta_call(a, da)
    dqs, dks, dvs, dbs = [], [], [], []
    for g in range(N_GROUPS):
        dq, dk, dv, db = _attn_bwd_call(proj, bias, da, lse, delta, g)
        dqs.append(dq)
        dks.append(dk)
        dvs.append(dv)
        dbs.append(db)
    G["rel_bias"] = _bias_grad_call(jnp.concatenate(dbs, axis=0), buckets)
    dproj = _dproj_call(dqs + dks + dvs, [dval, dgate, dga, dgc])
    dproj = on_grad("w_in", _matmul(h1, dproj, "tn", "proj_in_dw", out_shards=True), carry=dproj)
    dh1 = _matmul(dproj, P["w_in"], "nt", "proj_in_dx")
    dh1 = on_grad(None, None, carry=dh1)
    grad_x, G["norm_mix_pre"] = _in_bwd_call(xs, P["norm_mix_pre"], dh1, dx1)
    return loss_tile, grad_x, G


def kernel(x, w_in, b_gate, rel_bias, w_attn_out, conv_dw_w, conv_dw_b, conv_ln_g, conv_ln_b, conv_pw_w, w_out, norm_mix_pre, norm_mix_post, norm_ffn_pre, norm_ffn_post, w_up, ffn_conv_w, ffn_conv_b, w_down, loss_target, m_w_in, m_b_gate, m_rel_bias, m_w_attn_out, m_conv_dw_w, m_conv_dw_b, m_conv_ln_g, m_conv_ln_b, m_conv_pw_w, m_w_out, m_norm_mix_pre, m_norm_mix_post, m_norm_ffn_pre, m_norm_ffn_post, m_w_up, m_ffn_conv_w, m_ffn_conv_b, m_w_down, v_w_in, v_b_gate, v_rel_bias, v_w_attn_out, v_conv_dw_w, v_conv_dw_b, v_conv_ln_g, v_conv_ln_b, v_conv_pw_w, v_w_out, v_norm_mix_pre, v_norm_mix_post, v_norm_ffn_pre, v_norm_ffn_post, v_w_up, v_ffn_conv_w, v_ffn_conv_b, v_w_down):
    weights = dict(w_in=w_in, b_gate=b_gate, rel_bias=rel_bias, w_attn_out=w_attn_out, conv_dw_w=conv_dw_w, conv_dw_b=conv_dw_b,
                   conv_ln_g=conv_ln_g, conv_ln_b=conv_ln_b, conv_pw_w=conv_pw_w, w_out=w_out, norm_mix_pre=norm_mix_pre,
                   norm_mix_post=norm_mix_post, norm_ffn_pre=norm_ffn_pre, norm_ffn_post=norm_ffn_post, w_up=w_up,
                   ffn_conv_w=ffn_conv_w, ffn_conv_b=ffn_conv_b, w_down=w_down)
    m_in = dict(w_in=m_w_in, b_gate=m_b_gate, rel_bias=m_rel_bias, w_attn_out=m_w_attn_out, conv_dw_w=m_conv_dw_w,
                conv_dw_b=m_conv_dw_b, conv_ln_g=m_conv_ln_g, conv_ln_b=m_conv_ln_b, conv_pw_w=m_conv_pw_w, w_out=m_w_out,
                norm_mix_pre=m_norm_mix_pre, norm_mix_post=m_norm_mix_post, norm_ffn_pre=m_norm_ffn_pre,
                norm_ffn_post=m_norm_ffn_post, w_up=m_w_up, ffn_conv_w=m_ffn_conv_w, ffn_conv_b=m_ffn_conv_b, w_down=m_w_down)
    v_in = dict(w_in=v_w_in, b_gate=v_b_gate, rel_bias=v_rel_bias, w_attn_out=v_w_attn_out, conv_dw_w=v_conv_dw_w,
                conv_dw_b=v_conv_dw_b, conv_ln_g=v_conv_ln_g, conv_ln_b=v_conv_ln_b, conv_pw_w=v_conv_pw_w, w_out=v_w_out,
                norm_mix_pre=v_norm_mix_pre, norm_mix_post=v_norm_mix_post, norm_ffn_pre=v_norm_ffn_pre,
                norm_ffn_post=v_norm_ffn_post, w_up=v_w_up, ffn_conv_w=v_ffn_conv_w, ffn_conv_b=v_ffn_conv_b, w_down=v_w_down)
    names = list(weights)
    xi, yi, ci = _position()
    chip = 2 * xi + yi
    core_arr = jnp.reshape(ci, (1,)).astype(jnp.int32)

    xs = x[0]
    target = loss_target[0]
    S, D = xs.shape

    big = ["w_in", "w_attn_out", "conv_pw_w", "w_out", "w_up", "w_down"]
    row_sharded = ("conv_pw_w", "w_out", "w_down")
    natural = lambda k, g: g.reshape(-1, g.shape[2]) if k in row_sharded else g
    first_srcs = [w_in[0].astype(BF16), conv_dw_w[0], ffn_conv_w[0]]
    first_lands = [lax.empty((N_CHIPS,) + s.shape, s.dtype) for s in first_srcs]
    (first_hop,) = _split_start("gather_in_start", [(first_srcs, first_lands, 4 * len(first_srcs), _first_hop_copies)], core_arr)
    launched = first_hop["token"]
    late_sets = dict(mix=["w_attn_out", "conv_pw_w", "w_out"], up=["w_up"], down=["w_down"])
    late_groups = []
    for keys in late_sets.values():
        srcs = [(weights[k][0] + launched).astype(BF16) for k in keys]
        late_groups.append((srcs, [lax.empty((N_CHIPS,) + s.shape, BF16) for s in srcs], 4 * len(keys), _gather_copies))
    started = {}

    def late_weights(tag, after):
        if tag == "in":
            w_in_halves, dw4, fc4 = _split_wait("gather_in_wait", first_hop, _first_hop_copies, after)[len(first_srcs):]
            second_hop, *late = _split_start("gather_in_pass_start", [([], [w_in_halves], 3, _second_hop_copies)] + late_groups, dw4)
            started.update(zip(late_sets, late))
            (w_in_full,) = _split_wait("gather_in_pass_wait", second_hop, _second_hop_copies, second_hop["tile"])
            return dict(w_in=w_in_full, conv_dw_w=jnp.concatenate(list(dw4), axis=1), ffn_conv_w=jnp.concatenate(list(fc4), axis=1))
        landed = _split_wait(f"gather_{tag}_wait", started[tag], _gather_copies, after)[len(late_sets[tag]):]
        return {k: natural(k, g) for k, g in zip(late_sets[tag], landed)}

    chip_core = jnp.stack([chip, ci]).astype(jnp.int32)
    exchanging, pending = {}, {}

    held = []

    def launch(tag, after, carry=None):
        keys, groups, partial = [], [], {}
        for k in list(exchanging):
            gk, r1 = _split_wait(f"sibling_exchange_wait_{k}", exchanging.pop(k), _sibling_copies, after)
            partial[k], s16 = _pair_sum_call(gk, r1, core_arr, f"pair_sum_{k}")
            keys.append(k)
            groups.append(([s16], [lax.empty((3,) + s16.shape[1:], BF16)], 3, _exchange_copies))
        fresh = [k for k, _ in held]
        for _, g3 in held:
            groups.append(([g3], [lax.empty((N_CHIPS, g3.shape[1] // 2, g3.shape[2]), F32)], 1, _sibling_copies))
        held.clear()
        begun = _split_start(f"grad_exchange_start_{tag}", groups, core_arr, carry)
        for k, st in zip(keys + fresh, begun):
            if k in partial:
                pending[k] = (partial[k], st)
            else:
                exchanging[k] = st
        return begun[0]["carry"]

    def on_grad(k, g3, carry=None):
        if k is None:
            return launch("last", carry[:SUBLANES, :LANES], carry)
        held.append((k, g3))
        if k in ("w_down", "w_out", "w_attn_out"):
            return carry
        return launch(k, g3[0, :SUBLANES, :LANES], carry)

    def finish(keys, after, tag):
        halves = []
        for k in keys:
            s32, st = pending[k]
            recv2 = _split_wait(f"chip_exchange_wait_{k}", st, _exchange_copies, after)[1]
            halves.append(_chip_sum_call(s32, recv2, chip_core, f"chip_sum_{k}"))
        return dict(zip(keys, _sibling_assemble_call(halves, f"grad_sibling_assemble_{tag}")))

    P = dict(b_gate=b_gate, rel_bias=rel_bias, conv_dw_b=conv_dw_b, conv_ln_g=conv_ln_g, conv_ln_b=conv_ln_b,
             norm_mix_pre=norm_mix_pre + launched, norm_mix_post=norm_mix_post, norm_ffn_pre=norm_ffn_pre,
             norm_ffn_post=norm_ffn_post, ffn_conv_b=ffn_conv_b)
    loss_tile, grad_x, G = _local_step(xs, target, P, late_weights, on_grad)

    small = [k for k in names if k not in big]
    packed = _pack([loss_tile[:1]] + [G[k] for k in small])
    (allsum,) = _split_start("small_grad_allsum_start",
                             [([packed], [jnp.zeros((N_DEVICES,) + packed.shape, F32)], N_DEVICES - 1, _allsum_copies)], core_arr)

    reduced, grads, deltas, new_m, new_v = {}, {}, {}, {}, {}

    def update(keys):
        for k in keys:
            gk, d, mn, vn = _adamw_call(weights[k][0], reduced[k], m_in[k][0], v_in[k][0], f"adamw_{k}")
            grads[k], deltas[k], new_m[k], new_v[k] = gk[None], d[None], mn[None], vn[None]

    others = [k for k in big if k != "w_in"]
    reduced.update(finish(others, allsum["tile"], "others"))
    update(others)
    reduced.update(finish(["w_in"], deltas["w_up"], "w_in"))
    update(["w_in"])

    me_chip = jnp.stack([4 * xi + 2 * yi + ci, chip]).astype(jnp.int32)
    mine, landed = _split_wait("small_grad_allsum_wait", allsum, _allsum_copies, deltas["w_in"])
    piece_shapes = [(1, LANES)] + [(G[k].size // LANES, LANES) if k == "rel_bias" else G[k].shape for k in small]
    piece_cols = [0] + [weights[k].shape[2] if k in ("conv_dw_w", "ffn_conv_w") else 0 for k in small]
    loss_row, *summed = _ordered_sum_call(mine, landed, me_chip, piece_shapes, piece_cols)
    loss = loss_row[0, 0]
    for k, gsum in zip(small, summed):
        grads[k] = gsum.reshape(weights[k].shape)
    ds, mns, vns = _adamw_small_call([weights[k] for k in small], [grads[k] for k in small],
                                     [m_in[k] for k in small], [v_in[k] for k in small])
    deltas.update(zip(small, ds))
    new_m.update(zip(small, mns))
    new_v.update(zip(small, vns))

    return (loss, grad_x[None], *[grads[k] for k in names], *[deltas[k] for k in names],
            *[new_m[k] for k in names], *[new_v[k] for k in names])
```

```python
import functools
import math

import jax
import jax.numpy as jnp
import numpy as np
from jax import lax
from jax.experimental import pallas as pl
from jax.experimental.pallas import tpu as pltpu

F32 = jnp.float32
BF16 = jnp.bfloat16
MESH = pl.DeviceIdType.MESH

HEAD_DIM = 128
HEADS_PER_GROUP = 4
DILATED_PATTERNS = ((128, 1), (512, 4), (2048, 16))
N_GROUPS = 3
N_HEADS = N_GROUPS * HEADS_PER_GROUP
SPAN = 128
GROUP_WIDTH = HEADS_PER_GROUP * HEAD_DIM
CONV_WIDTH = 31
FFN_CONV_WIDTH = 3
N_BUCKETS = 32
MAX_DISTANCE = 2048
RMS_EPS = 1e-6
LN_EPS = 1e-5
NEG_INF = -1e30
ADAM_LR = 0.001
ADAM_B1 = 0.9
ADAM_B2 = 0.999
ADAM_EPS = 1e-08
ADAM_WD = 0.01
ADAM_STEP = 10

LANES = 128
SUBLANES = 8
PACKED_ROWS = 16
ROW_TILE = 512
GATE_ROWS, GATE_COLS = 512, 512
TIME_BLOCK = 128
CONV_PAD = 32
FFN_PAD = 8
VMEM_LIMIT = 56 << 20


def _params(sem=None, vmem=None):
    kw = {}
    if sem is not None:
        kw["dimension_semantics"] = sem
    if vmem is not None:
        kw["vmem_limit_bytes"] = vmem
    return pltpu.CompilerParams(**kw)


def _pick(n, cands):
    for c in cands:
        if n % c == 0:
            return c
    return n


ELEMENTWISE_TILE_BYTES = 3 << 19


def _row_tile(rows, cols):
    for align in (16, SUBLANES):
        fits = [t for t in range(align, rows + 1, align) if rows % t == 0 and t * cols * 4 <= ELEMENTWISE_TILE_BYTES]
        if fits:
            return max(fits)
    return SUBLANES


N_CHIPS = 4
M_TILES = (1024, 1408, 512, 256, 128)
N_TILES = (1024, 512, 1408, 256, 128)
K_TILES = (2176, 2048, 1408, 1024, 512, 256, 128)


def _matmul(a, b, mode, name, out_shards=False, tm=None):
    assert a.dtype == BF16 and b.dtype == BF16, (name, a.dtype, b.dtype)
    b3 = b.ndim == 3
    tn = tk = None
    halves = None
    if mode == "nn":
        M, K = a.shape
        N = b.shape[-1] * (N_CHIPS if b3 else 1)
        tn = b.shape[-1] if b3 else None
    elif mode == "nt":
        if a.ndim == 3:
            halves = a.shape[2]
        M, K = a.shape[-2], a.shape[-1] * (a.shape[0] if a.ndim == 3 else 1)
        N = b.shape[-2]
        tk = b.shape[-1] if b3 else None
    else:
        if b3:
            halves = b.shape[2]
        K, M = a.shape
        N = b.shape[-1] * (b.shape[0] if b3 else 1)
        tn = N // N_CHIPS if out_shards else None
    tm = tm or _pick(M, M_TILES)
    tn = tn or _pick(N, N_TILES)
    tk = tk or _pick(K, K_TILES)
    nk = K // tk
    dn = {"nn": (((1,), (0,)), ((), ())), "nt": (((1,), (1,)), ((), ())), "tn": (((0,), (0,)), ((), ()))}[mode]

    def body(a_ref, b_ref, o_ref):
        if nk == 1:
            o_ref[...] = lax.dot_general(a_ref[...], b_ref[...], dn, preferred_element_type=F32)
        else:
            @pl.when(pl.program_id(2) == 0)
            def _():
                o_ref[...] = jnp.zeros_like(o_ref)

            o_ref[...] += lax.dot_general(a_ref[...], b_ref[...], dn, preferred_element_type=F32)

    if mode == "tn":
        a_spec = pl.BlockSpec((tk, tm), lambda i, j, k: (k, i))
    elif halves:
        per = halves // tk
        a_spec = pl.BlockSpec((None, tm, tk), lambda i, j, k: (k // per, i, k % per))
    else:
        a_spec = pl.BlockSpec((tm, tk), lambda i, j, k: (i, k))
    if mode == "nn":
        b_spec = pl.BlockSpec((None, tk, tn), lambda i, j, k: (j, k, 0)) if b3 else pl.BlockSpec((tk, tn), lambda i, j, k: (k, j))
    elif mode == "nt":
        b_spec = pl.BlockSpec((None, tn, tk), lambda i, j, k: (k, j, 0)) if b3 else pl.BlockSpec((tn, tk), lambda i, j, k: (j, k))
    elif halves:
        per = halves // tn
        b_spec = pl.BlockSpec((None, tk, tn), lambda i, j, k: (j // per, k, j % per))
    else:
        b_spec = pl.BlockSpec((tk, tn), lambda i, j, k: (k, j))
    if out_shards:
        out_spec = pl.BlockSpec((None, tm, tn), lambda i, j, k: (j, i, 0))
        out_shape = jax.ShapeDtypeStruct((N_CHIPS, M, tn), F32)
    else:
        out_spec = pl.BlockSpec((tm, tn), lambda i, j, k: (i, j))
        out_shape = jax.ShapeDtypeStruct((M, N), F32)
    return pl.pallas_call(
        body, name=name, grid=(M // tm, N // tn, nk),
        in_specs=[a_spec, b_spec], out_specs=out_spec, out_shape=out_shape,
        compiler_params=_params(("parallel", "parallel", "arbitrary"), VMEM_LIMIT),
    )(a, b)


def _matmul_shards(a, b3, ids, into, name):
    M, K = a.shape
    Cs = b3.shape[2]
    n = ids.shape[0] // 2
    tm = _pick(M, M_TILES)

    def body(ids_ref, a_ref, b_ref, *rest):
        rest[-1][...] = lax.dot_general(a_ref[...], b_ref[...], (((1,), (0,)), ((), ())), preferred_element_type=F32)

    in_specs = [pl.BlockSpec((tm, K), lambda i, j, ids_ref: (i, 0)),
                pl.BlockSpec((None, K, Cs), lambda i, j, ids_ref: (ids_ref[j], 0, 0))]
    operands = [ids, a, b3]
    if into is not None:
        in_specs.append(ANY)
        operands.append(into)
    return pl.pallas_call(
        body, name=name,
        grid_spec=pltpu.PrefetchScalarGridSpec(
            num_scalar_prefetch=1, grid=(M // tm, n), in_specs=in_specs,
            out_specs=pl.BlockSpec((tm, Cs), lambda i, j, ids_ref: (i, ids_ref[n + j]))),
        out_shape=jax.ShapeDtypeStruct((M, N_CHIPS * Cs), F32),
        input_output_aliases={} if into is None else {3: 0},
        compiler_params=_params(("parallel", "parallel"), VMEM_LIMIT),
    )(*operands)


def _rms(x, g):
    r = lax.rsqrt(jnp.mean(x * x, axis=-1, keepdims=True) + RMS_EPS)
    return x * r * g


def _rms_bwd(x, g, dy):
    r = lax.rsqrt(jnp.mean(x * x, axis=-1, keepdims=True) + RMS_EPS)
    n = x * r
    dn = dy * g
    dx = r * (dn - n * jnp.mean(dn * n, axis=-1, keepdims=True))
    return dx, jnp.sum(dy * n, axis=0, keepdims=True)


def _sigmoid(x):
    return 1.0 / (1.0 + jnp.exp(-x))


_GELU_C = math.sqrt(2.0 / math.pi)


def _gelu(x):
    return 0.5 * x * (1.0 + jnp.tanh(_GELU_C * (x + 0.044715 * x * x * x)))


def _gelu_and_grad(x):
    x2 = x * x
    t = jnp.tanh(_GELU_C * x * (1.0 + 0.044715 * x2))
    half = 0.5 * (1.0 + t)
    return x * half, half + (0.5 * _GELU_C) * x * (1.0 - t * t) * (1.0 + (3.0 * 0.044715) * x2)


def _row_spec(width, col_block=0):
    return pl.BlockSpec((ROW_TILE, width), lambda i: (i, col_block))


def _vec_spec(width, col_block=0):
    return pl.BlockSpec((1, width), lambda i: (0, col_block))


def _accumulate(ref, part):
    @pl.when(pl.program_id(0) == 0)
    def _():
        ref[...] = part

    @pl.when(pl.program_id(0) > 0)
    def _():
        ref[...] += part


def _rms_fwd_call(x, g):
    S, D = x.shape

    def body(x_ref, g_ref, h_ref):
        h_ref[...] = _rms(x_ref[...], g_ref[...]).astype(BF16)

    return pl.pallas_call(
        body, name="rms_mix_pre", grid=(S // ROW_TILE,),
        in_specs=[_row_spec(D), _vec_spec(D)], out_specs=_row_spec(D),
        out_shape=jax.ShapeDtypeStruct((S, D), BF16),
        compiler_params=_params(("parallel",)),
    )(x, g)


def _ln_silu_call(c1, g, b):
    S, C = c1.shape

    def body(c_ref, g_ref, b_ref, o_ref):
        xv = c_ref[...]
        mu = jnp.mean(xv, axis=-1, keepdims=True)
        xc = xv - mu
        var = jnp.mean(xc * xc, axis=-1, keepdims=True)
        z = xc * lax.rsqrt(var + LN_EPS) * g_ref[...] + b_ref[...]
        o_ref[...] = (z * _sigmoid(z)).astype(BF16)

    return pl.pallas_call(
        body, name="conv_ln_silu", grid=(S // ROW_TILE,),
        in_specs=[_row_spec(C), _vec_spec(C), _vec_spec(C)], out_specs=_row_spec(C),
        out_shape=jax.ShapeDtypeStruct((S, C), BF16),
        compiler_params=_params(("parallel",)),
    )(c1, g, b)


def _ln_silu_bwd_call(c1, g, b, dc):
    S, C = c1.shape

    def body(c_ref, g_ref, b_ref, dc_ref, dx_ref, dg_ref, db_ref):
        xv = c_ref[...]
        mu = jnp.mean(xv, axis=-1, keepdims=True)
        xc = xv - mu
        rs = lax.rsqrt(jnp.mean(xc * xc, axis=-1, keepdims=True) + LN_EPS)
        xh = xc * rs
        z = xh * g_ref[...] + b_ref[...]
        sg = _sigmoid(z)
        dz = dc_ref[...] * (sg * (1.0 + z * (1.0 - sg)))
        dxh = dz * g_ref[...]
        dx_ref[...] = rs * (dxh - jnp.mean(dxh, axis=-1, keepdims=True) - xh * jnp.mean(dxh * xh, axis=-1, keepdims=True))
        _accumulate(dg_ref, jnp.sum(dz * xh, axis=0, keepdims=True))
        _accumulate(db_ref, jnp.sum(dz, axis=0, keepdims=True))

    return pl.pallas_call(
        body, name="conv_ln_silu_bwd", grid=(S // ROW_TILE,),
        in_specs=[_row_spec(C), _vec_spec(C), _vec_spec(C), _row_spec(C)],
        out_specs=[_row_spec(C), _vec_spec(C), _vec_spec(C)],
        out_shape=[jax.ShapeDtypeStruct((S, C), F32), jax.ShapeDtypeStruct((1, C), F32), jax.ShapeDtypeStruct((1, C), F32)],
        compiler_params=_params(("arbitrary",)),
    )(c1, g, b, dc)


def _mix_call(proj, gate_col0, b_gate, y_a, y_c):
    S, D = y_a.shape
    w = GATE_COLS
    nc = D // w
    ga0, gc0 = gate_col0 // w, (gate_col0 + D) // w

    def body(ga_ref, gc_ref, ba_ref, bc_ref, ya_ref, yc_ref, o_ref):
        o_ref[...] = (_sigmoid(ga_ref[...] + ba_ref[...]) * ya_ref[...]
                      + _sigmoid(gc_ref[...] + bc_ref[...]) * yc_ref[...]).astype(BF16)

    tile = lambda off: pl.BlockSpec((GATE_ROWS, w), lambda i, j: (i, off + j))
    vec = lambda off: pl.BlockSpec((1, w), lambda i, j: (0, off + j))
    return pl.pallas_call(
        body, name="gate_mix", grid=(S // GATE_ROWS, nc),
        in_specs=[tile(ga0), tile(gc0), vec(0), vec(nc), tile(0), tile(0)],
        out_specs=tile(0), out_shape=jax.ShapeDtypeStruct((S, D), BF16),
        compiler_params=_params(("parallel", "parallel")),
    )(proj, proj, b_gate, b_gate, y_a, y_c)


def _mix_bwd_call(dmixed, proj, gate_col0, b_gate, y_a, y_c):
    S, D = y_a.shape
    w = GATE_COLS
    nc = D // w
    ga0, gc0 = gate_col0 // w, (gate_col0 + D) // w

    def body(dm_ref, ga_ref, gc_ref, ba_ref, bc_ref, ya_ref, yc_ref, dya_ref, dyc_ref, dga_ref, dgc_ref, dba_ref, dbc_ref):
        dm = dm_ref[...]
        sa = _sigmoid(ga_ref[...] + ba_ref[...])
        sc = _sigmoid(gc_ref[...] + bc_ref[...])
        dya_ref[...] = (dm * sa).astype(BF16)
        dyc_ref[...] = (dm * sc).astype(BF16)
        dga = dm * ya_ref[...] * sa * (1.0 - sa)
        dgc = dm * yc_ref[...] * sc * (1.0 - sc)
        dga_ref[...] = dga.astype(BF16)
        dgc_ref[...] = dgc.astype(BF16)
        pa = jnp.sum(dga, axis=0, keepdims=True)
        pc = jnp.sum(dgc, axis=0, keepdims=True)

        @pl.when(pl.program_id(1) == 0)
        def _():
            dba_ref[...] = pa
            dbc_ref[...] = pc

        @pl.when(pl.program_id(1) > 0)
        def _():
            dba_ref[...] += pa
            dbc_ref[...] += pc

    tile = lambda off: pl.BlockSpec((GATE_ROWS, w), lambda j, i: (i, off + j))
    vec = lambda off: pl.BlockSpec((1, w), lambda j, i: (0, off + j))
    return pl.pallas_call(
        body, name="gate_mix_bwd", grid=(nc, S // GATE_ROWS),
        in_specs=[tile(0), tile(ga0), tile(gc0), vec(0), vec(nc), tile(0), tile(0)],
        out_specs=[tile(0), tile(0), tile(0), tile(0), vec(0), vec(0)],
        out_shape=[jax.ShapeDtypeStruct((S, D), BF16)] * 4 + [
                   jax.ShapeDtypeStruct((1, D), F32), jax.ShapeDtypeStruct((1, D), F32)],
        compiler_params=_params(("parallel", "arbitrary")),
    )(dmixed, proj, proj, b_gate, b_gate, y_a, y_c)


def _res1_call(x, out, g_post, g_pre):
    S, D = x.shape

    def body(x_ref, o_ref, gp_ref, gq_ref, x1_ref, h2_ref):
        x1 = x_ref[...] + _rms(o_ref[...], gp_ref[...])
        x1_ref[...] = x1
        h2_ref[...] = _rms(x1, gq_ref[...]).astype(BF16)

    return pl.pallas_call(
        body, name="residual_mix", grid=(S // ROW_TILE,),
        in_specs=[_row_spec(D), _row_spec(D), _vec_spec(D), _vec_spec(D)],
        out_specs=[_row_spec(D), _row_spec(D)],
        out_shape=[jax.ShapeDtypeStruct((S, D), F32), jax.ShapeDtypeStruct((S, D), BF16)],
        compiler_params=_params(("parallel",)),
    )(x, out, g_post, g_pre)


def _loss_call(y, x1, g_post, target):
    S, D = y.shape

    def body(y_ref, x1_ref, g_ref, t_ref, loss_ref, dx_ref, dy_ref, dg_ref):
        yv, gv = y_ref[...], g_ref[...]
        err = x1_ref[...] + _rms(yv, gv) - t_ref[...]
        dx2 = err * (1.0 / D)
        dx_ref[...] = dx2
        dy, dg = _rms_bwd(yv, gv, dx2)
        dy_ref[...] = dy.astype(BF16)
        _accumulate(dg_ref, dg)
        part = 0.5 * jnp.sum(jnp.mean(err * err, axis=-1, keepdims=True), axis=0, keepdims=True)
        _accumulate(loss_ref, jnp.broadcast_to(part, (SUBLANES, LANES)))

    return pl.pallas_call(
        body, name="residual_ffn_loss", grid=(S // ROW_TILE,),
        in_specs=[_row_spec(D), _row_spec(D), _vec_spec(D), _row_spec(D)],
        out_specs=[pl.BlockSpec((SUBLANES, LANES), lambda i: (0, 0)), _row_spec(D), _row_spec(D), _vec_spec(D)],
        out_shape=[jax.ShapeDtypeStruct((SUBLANES, LANES), F32), jax.ShapeDtypeStruct((S, D), F32),
                   jax.ShapeDtypeStruct((S, D), BF16), jax.ShapeDtypeStruct((1, D), F32)],
        compiler_params=_params(("arbitrary",)),
    )(y, x1, g_post, target)


def _mid_bwd_call(x1, g_pre, dh2, dx2, out, g_post):
    S, D = x1.shape

    def body(x1_ref, gq_ref, dh_ref, dx2_ref, o_ref, gp_ref, dx1_ref, do_ref, dgq_ref, dgp_ref):
        d, dgq = _rms_bwd(x1_ref[...], gq_ref[...], dh_ref[...])
        dx1 = dx2_ref[...] + d
        dx1_ref[...] = dx1
        do, dgp = _rms_bwd(o_ref[...], gp_ref[...], dx1)
        do_ref[...] = do.astype(BF16)
        _accumulate(dgq_ref, dgq)
        _accumulate(dgp_ref, dgp)

    return pl.pallas_call(
        body, name="residual_mix_bwd", grid=(S // ROW_TILE,),
        in_specs=[_row_spec(D), _vec_spec(D), _row_spec(D), _row_spec(D), _row_spec(D), _vec_spec(D)],
        out_specs=[_row_spec(D), _row_spec(D), _vec_spec(D), _vec_spec(D)],
        out_shape=[jax.ShapeDtypeStruct((S, D), F32), jax.ShapeDtypeStruct((S, D), BF16)] + [jax.ShapeDtypeStruct((1, D), F32)] * 2,
        compiler_params=_params(("arbitrary",)),
    )(x1, g_pre, dh2, dx2, out, g_post)


def _in_bwd_call(x, g, dh1, dx1):
    S, D = x.shape

    def body(x_ref, g_ref, dh_ref, dx1_ref, gx_ref, dg_ref):
        d, dg = _rms_bwd(x_ref[...], g_ref[...], dh_ref[...])
        gx_ref[...] = dx1_ref[...] + d
        _accumulate(dg_ref, dg)

    return pl.pallas_call(
        body, name="rms_mix_pre_bwd", grid=(S // ROW_TILE,),
        in_specs=[_row_spec(D), _vec_spec(D), _row_spec(D), _row_spec(D)],
        out_specs=[_row_spec(D), _vec_spec(D)],
        out_shape=[jax.ShapeDtypeStruct((S, D), F32), jax.ShapeDtypeStruct((1, D), F32)],
        compiler_params=_params(("arbitrary",)),
    )(x, g, dh1, dx1)


def _bucket_table(dilation):
    qi = np.arange(SPAN)[:, None]
    ki = np.arange(2 * SPAN)[None, :]
    dist = np.maximum(qi + SPAN - ki, 0) * dilation
    max_exact = N_BUCKETS // 2
    d = np.maximum(dist, 1).astype(np.float64)
    large = max_exact + (np.log(d / max_exact) / math.log(MAX_DISTANCE / max_exact) * (N_BUCKETS - max_exact)).astype(np.int32)
    large = np.minimum(large, N_BUCKETS - 1)
    return np.where(dist < max_exact, dist, large).astype(np.int32)


def _bucket_tables():
    return jnp.asarray(np.stack([_bucket_table(r) for _, r in DILATED_PATTERNS]))


def _bias_table_call(rel_bias, buckets):
    def body(rb_ref, bk_ref, o_ref):
        for h in range(N_HEADS):
            bk = bk_ref[h // HEADS_PER_GROUP]

            def step(b, acc):
                return jnp.where(bk == b, rb_ref[b, h], acc)

            o_ref[h] = lax.fori_loop(0, N_BUCKETS, step, jnp.zeros((SPAN, 2 * SPAN), F32))

    return pl.pallas_call(
        body, name="rel_bias_table",
        in_specs=[pl.BlockSpec(memory_space=pltpu.SMEM), pl.BlockSpec(memory_space=pltpu.VMEM)],
        out_specs=pl.BlockSpec(memory_space=pltpu.VMEM),
        out_shape=jax.ShapeDtypeStruct((N_HEADS, SPAN, 2 * SPAN), F32),
    )(rel_bias, buckets)


def _bias_grad_call(dbias, buckets):
    def body(db_ref, bk_ref, o_ref, rows_ref):
        for h in range(N_HEADS):
            bk = bk_ref[h // HEADS_PER_GROUP]
            dv = db_ref[h]

            def step(b, carry):
                rows_ref[h, b] = jnp.sum(jnp.where(bk == b, dv, 0.0), axis=0, keepdims=True)
                return carry

            lax.fori_loop(0, N_BUCKETS, step, 0)
        o_ref[...] = jnp.sum(rows_ref[...], axis=-1, keepdims=True)

    out = pl.pallas_call(
        body, name="rel_bias_grad",
        in_specs=[pl.BlockSpec(memory_space=pltpu.VMEM), pl.BlockSpec(memory_space=pltpu.VMEM)],
        out_specs=pl.BlockSpec(memory_space=pltpu.VMEM),
        out_shape=jax.ShapeDtypeStruct((N_HEADS, N_BUCKETS, 1, 1), F32),
        scratch_shapes=[pltpu.VMEM((N_HEADS, N_BUCKETS, 1, 2 * SPAN), F32)],
    )(dbias, buckets)
    return out.reshape(N_HEADS, N_BUCKETS).T


def _dot_nt(a, b):
    return lax.dot_general(a, b, (((1,), (1,)), ((), ())), preferred_element_type=F32)


def _dot_nn(a, b):
    return lax.dot_general(a, b, (((1,), (0,)), ((), ())), preferred_element_type=F32)


def _dot_tn(a, b):
    return lax.dot_general(a, b, (((0,), (0,)), ((), ())), preferred_element_type=F32)


def _band_masks(n, nb):
    qi = lax.broadcasted_iota(jnp.int32, (SPAN, SPAN), 0)
    ki = lax.broadcasted_iota(jnp.int32, (SPAN, SPAN), 1)
    prev_ok = jnp.logical_and(ki >= qi, n > 0)
    cur_ok = ki <= qi
    next_ok = jnp.logical_and(ki >= qi, n < nb - 1)
    return prev_ok, cur_ok, next_ok


def _wide_band_mask(n):
    qi = lax.broadcasted_iota(jnp.int32, (SPAN, 2 * SPAN), 0)
    ki = lax.broadcasted_iota(jnp.int32, (SPAN, 2 * SPAN), 1)
    prev_ok = jnp.logical_and(jnp.logical_and(ki < SPAN, ki >= qi), n > 0)
    cur_ok = jnp.logical_and(ki >= SPAN, ki - SPAN <= qi)
    return jnp.logical_or(prev_ok, cur_ok)


def _attn_plan(S, group):
    r = DILATED_PATTERNS[group][1]
    hp, per = (HEADS_PER_GROUP, 1) if r == 1 else (2, 4)
    return r, S // (r * SPAN), hp, per


def _residue_rows(rho, r):
    return slice(None) if r == 1 else pl.ds(rho, SPAN, stride=r)


def _for_residues(r, per, fn):
    if r == per:
        for u in range(per):
            fn(u)
        return

    def step(i, carry):
        for u in range(per):
            fn(i * per + u)
        return carry

    lax.fori_loop(0, r // per, step, 0)


def _attn_fwd_call(proj, bias, group):
    S = proj.shape[0]
    r, nb, hp, per = _attn_plan(S, group)
    scale = HEAD_DIM ** -0.5
    kinds = ("q", "kp", "kc", "vp", "vc") if nb > 1 else ("q", "kc", "vc")

    per_kind = _refs_per_kind(r, hp)

    def body(*refs):
        ins = {kind: refs[i * per_kind:(i + 1) * per_kind] for i, kind in enumerate(kinds)}
        b_ref, o_ref, lse_ref = refs[len(kinds) * per_kind:]
        n = pl.program_id(1)
        prev_ok, cur_ok, _ = _band_masks(n, nb)

        band_ok = _wide_band_mask(n) if nb > 1 else cur_ok

        def residue(rho):
            rows = _residue_rows(rho, r)
            for j in range(hp):
                get = lambda kind: _head_rows(ins[kind], j, rows, r).astype(BF16)
                q = get("q")
                if nb > 1:
                    keys, vals, bias_j = jnp.concatenate([get("kp"), get("kc")], axis=0), jnp.concatenate([get("vp"), get("vc")], axis=0), b_ref[j]
                else:
                    keys, vals, bias_j = get("kc"), get("vc"), b_ref[j, :, SPAN:]
                s = jnp.where(band_ok, _dot_nt(q, keys) * scale + bias_j, NEG_INF)
                m = jnp.max(s, axis=-1, keepdims=True)
                p = jnp.exp(s - m)
                den = jnp.sum(p, axis=-1, keepdims=True)
                o_ref[j, rows, :] = _dot_nn(p.astype(BF16), vals) / den
                lse_ref[j, rows, :] = jnp.broadcast_to(m + jnp.log(den), (SPAN, HEAD_DIM))

        _for_residues(r, per, residue)

    in_specs = [_head_spec(r, nb, hp, kind, group, jj) for kind in kinds for jj in range(per_kind)]
    in_specs.append(pl.BlockSpec((hp, SPAN, 2 * SPAN), lambda j, n: (group * (HEADS_PER_GROUP // hp) + j, 0, 0)))
    out = pl.BlockSpec((hp, r * SPAN, HEAD_DIM), lambda j, n: (j, n, 0))
    return pl.pallas_call(
        body, name=f"attn_fwd_g{group}", grid=(HEADS_PER_GROUP // hp, nb),
        in_specs=in_specs, out_specs=[out] * 2,
        out_shape=[jax.ShapeDtypeStruct((HEADS_PER_GROUP, S, HEAD_DIM), F32)] * 2,
        compiler_params=_params(("parallel", "parallel"), VMEM_LIMIT),
    )(*([proj] * (len(in_specs) - 1)), bias)


_PROJ_PART = dict(q=0, qn=0, kp=1, kc=1, vp=2, vc=2)


def _refs_per_kind(r, hp):
    return 1 if r == 1 else hp


def _head_rows(refs, j, rows, r):
    return refs[0][:, j * HEAD_DIM:(j + 1) * HEAD_DIM] if r == 1 else refs[j][rows, :]


def _head_spec(r, nb, hp, kind, group, jj):
    if kind in _PROJ_PART:
        base = (_PROJ_PART[kind] * N_GROUPS + group) * HEADS_PER_GROUP
    else:
        base = 0
    if kind.endswith("p"):
        row = lambda n: jnp.maximum(n - 1, 0)
    elif kind.endswith("n"):
        row = lambda n: jnp.minimum(n + 1, nb - 1)
    else:
        row = lambda n: n
    if r == 1:
        return pl.BlockSpec((SPAN, hp * HEAD_DIM), lambda j, n: (row(n), base // hp + j))
    return pl.BlockSpec((r * SPAN, HEAD_DIM), lambda j, n: (row(n), base + j * hp + jj))


def _attn_merge_call(parts):
    S = parts[0].shape[1]

    def body(o1, s1, o2, s2, o3, s3, a_ref, ab_ref, lse_ref):
        for j in range(HEADS_PER_GROUP):
            sl = slice(j * HEAD_DIM, (j + 1) * HEAD_DIM)
            mx = jnp.maximum(jnp.maximum(s1[j], s2[j]), s3[j])
            w1 = jnp.exp(s1[j] - mx)
            w2 = jnp.exp(s2[j] - mx)
            w3 = jnp.exp(s3[j] - mx)
            den = w1 + w2 + w3
            a = (w1 * o1[j] + w2 * o2[j] + w3 * o3[j]) / den
            a_ref[:, sl] = a
            ab_ref[:, sl] = a.astype(BF16)
            lse_ref[:, sl] = mx + jnp.log(den)

    heads = pl.BlockSpec((HEADS_PER_GROUP, ROW_TILE, HEAD_DIM), lambda i: (0, i, 0))
    return pl.pallas_call(
        body, name="attn_merge", grid=(S // ROW_TILE,),
        in_specs=[heads] * 6, out_specs=[_row_spec(GROUP_WIDTH)] * 3,
        out_shape=[jax.ShapeDtypeStruct((S, GROUP_WIDTH), F32), jax.ShapeDtypeStruct((S, GROUP_WIDTH), BF16),
                   jax.ShapeDtypeStruct((S, GROUP_WIDTH), F32)],
        compiler_params=_params(("parallel",)),
    )(*parts)


def _attn_delta_call(a, da):
    S = a.shape[0]

    def body(a_ref, da_ref, d_ref):
        for j in range(HEADS_PER_GROUP):
            sl = slice(j * HEAD_DIM, (j + 1) * HEAD_DIM)
            d = jnp.sum(a_ref[:, sl] * da_ref[:, sl], axis=-1, keepdims=True)
            d_ref[:, sl] = jnp.broadcast_to(d, (ROW_TILE, HEAD_DIM))

    return pl.pallas_call(
        body, name="attn_delta", grid=(S // ROW_TILE,),
        in_specs=[_row_spec(GROUP_WIDTH)] * 2, out_specs=_row_spec(GROUP_WIDTH),
        out_shape=jax.ShapeDtypeStruct((S, GROUP_WIDTH), F32),
        compiler_params=_params(("parallel",)),
    )(a, da)


def _attn_bwd_call(proj, bias, da, lse, delta, group):
    S = proj.shape[0]
    r, nb, hp, per = _attn_plan(S, group)
    scale = HEAD_DIM ** -0.5
    kinds = ("q", "qn", "kp", "kc", "vp", "vc", "da", "dan", "lse", "lsen", "dl", "dln") if nb > 1 else ("q", "kc", "vc", "da", "lse", "dl")
    source = dict(da=da, dan=da, lse=lse, lsen=lse, dl=delta, dln=delta)

    per_kind = _refs_per_kind(r, hp)

    def body(*refs):
        ins = {kind: refs[i * per_kind:(i + 1) * per_kind] for i, kind in enumerate(kinds)}
        b_ref, dq_ref, dk_ref, dv_ref, db_ref = refs[len(kinds) * per_kind:]
        n = pl.program_id(1)
        prev_ok, cur_ok, next_ok = _band_masks(n, nb)

        @pl.when(n == 0)
        def _():
            db_ref[...] = jnp.zeros_like(db_ref)

        band_ok = _wide_band_mask(n) if nb > 1 else cur_ok

        def residue(rho):
            rows = _residue_rows(rho, r)
            for j in range(hp):
                get = lambda kind: _head_rows(ins[kind], j, rows, r)
                q = get("q").astype(BF16)
                kc = get("kc").astype(BF16)
                vc = get("vc").astype(BF16)
                dav = get("da").astype(BF16)
                lse_q, dl_q = get("lse"), get("dl")
                if nb == 1:
                    pc = jnp.exp(jnp.where(cur_ok, _dot_nt(q, kc) * scale + b_ref[j, :, SPAN:], NEG_INF) - lse_q)
                    dsc = pc * (_dot_nt(dav, vc) - dl_q)
                    dsc_b = dsc.astype(BF16)
                    dq = _dot_nn(dsc_b, kc)
                    dk = _dot_tn(dsc_b, q)
                    dv = _dot_tn(pc.astype(BF16), dav)
                    db_ref[j, :, SPAN:] += dsc
                else:
                    qn = get("qn").astype(BF16)
                    dan = get("dan").astype(BF16)
                    keys = jnp.concatenate([get("kp").astype(BF16), kc], axis=0)
                    vals = jnp.concatenate([get("vp").astype(BF16), vc], axis=0)
                    wide = lambda t: jnp.concatenate([t, t], axis=1)
                    p = jnp.exp(jnp.where(band_ok, _dot_nt(q, keys) * scale + b_ref[j], NEG_INF) - wide(lse_q))
                    ds = p * (_dot_nt(dav, vals) - wide(dl_q))
                    dq = _dot_nn(ds.astype(BF16), keys)
                    db_ref[j] += ds
                    pn = jnp.exp(jnp.where(next_ok, _dot_nt(qn, kc) * scale + b_ref[j, :, :SPAN], NEG_INF) - get("lsen"))
                    dsn = pn * (_dot_nt(dan, vc) - get("dln"))
                    both = lambda cur_part, next_part: jnp.concatenate([cur_part.astype(BF16), next_part.astype(BF16)], axis=0)
                    dk = _dot_tn(both(ds[:, SPAN:], dsn), jnp.concatenate([q, qn], axis=0))
                    dv = _dot_tn(both(p[:, SPAN:], pn), jnp.concatenate([dav, dan], axis=0))
                dq_ref[j, rows, :] = dq * scale
                dk_ref[j, rows, :] = dk * scale
                dv_ref[j, rows, :] = dv

        _for_residues(r, per, residue)

    per_group = HEADS_PER_GROUP // hp
    band = (hp, SPAN, 2 * SPAN)
    in_specs = [_head_spec(r, nb, hp, kind, group, jj) for kind in kinds for jj in range(per_kind)]
    in_specs.append(pl.BlockSpec(band, lambda j, n: (group * per_group + j, 0, 0)))
    operands = [source.get(kind, proj) for kind in kinds for _ in range(per_kind)] + [bias]
    out = pl.BlockSpec((hp, r * SPAN, HEAD_DIM), lambda j, n: (j, n, 0))
    return pl.pallas_call(
        body, name=f"attn_bwd_g{group}", grid=(per_group, nb),
        in_specs=in_specs,
        out_specs=[out] * 3 + [pl.BlockSpec(band, lambda j, n: (j, 0, 0))],
        out_shape=[jax.ShapeDtypeStruct((HEADS_PER_GROUP, S, HEAD_DIM), F32)] * 3
        + [jax.ShapeDtypeStruct((HEADS_PER_GROUP, SPAN, 2 * SPAN), F32)],
        compiler_params=_params(("parallel", "arbitrary"), VMEM_LIMIT),
    )(*operands)


def _dproj_call(dqkv, tails):
    S = tails[0].shape[0]
    width = len(dqkv) * GROUP_WIDTH + sum(t.shape[1] for t in tails)

    def body(*refs):
        o_ref = refs[-1]
        col = 0
        for ref in refs[:len(dqkv)]:
            for j in range(HEADS_PER_GROUP):
                o_ref[:, col:col + HEAD_DIM] = ref[j].astype(BF16)
                col += HEAD_DIM
        for ref in refs[len(dqkv):-1]:
            o_ref[:, col:col + ref.shape[1]] = ref[...]
            col += ref.shape[1]

    heads = pl.BlockSpec((HEADS_PER_GROUP, ROW_TILE, HEAD_DIM), lambda i: (0, i, 0))
    return pl.pallas_call(
        body, name="dproj_assemble", grid=(S // ROW_TILE,),
        in_specs=[heads] * len(dqkv) + [_row_spec(t.shape[1]) for t in tails],
        out_specs=_row_spec(width), out_shape=jax.ShapeDtypeStruct((S, width), BF16),
        compiler_params=_params(("parallel",)),
    )(*dqkv, *tails)


def _tap_rows(xpad_ref, t0, k, width, pad):
    return xpad_ref[pl.ds(t0 + (pad - (width - 1 - k)), TIME_BLOCK), :]


def _conv_block(xpad_ref, t0, w_ref, width, pad):
    acc = None
    for k in range(width):
        term = w_ref[k:k + 1, :] * _tap_rows(xpad_ref, t0, k, width, pad)
        acc = term if acc is None else acc + term
    return acc


def _conv_transpose_block(dpad_ref, t0, w_ref, width):
    acc = None
    for k in range(width):
        term = w_ref[k:k + 1, :] * dpad_ref[pl.ds(t0 + (width - 1 - k), TIME_BLOCK), :]
        acc = term if acc is None else acc + term
    return acc


def _conv_weight_grad(xpad_ref, t0, dy, dw_ref, width, pad):
    for k in range(width):
        dw_ref[k:k + 1, :] += jnp.sum(dy * _tap_rows(xpad_ref, t0, k, width, pad), axis=0, keepdims=True)


def _time_loop(S, step, skip_first=0, skip_last=0):
    def it(tb, carry):
        step(pl.multiple_of(tb * TIME_BLOCK, TIME_BLOCK))
        return carry

    lax.fori_loop(skip_first, S // TIME_BLOCK - skip_last, it, 0)


def _fill_head(head_ref, x_ref, pad):
    head_ref[0:pad, :] = jnp.zeros((pad, LANES), F32)
    head_ref[pad:, :] = x_ref[0:TIME_BLOCK, :]


def _fill_tail(tail_ref, x_ref, pad):
    S = x_ref.shape[0]
    tail_ref[0:TIME_BLOCK, :] = x_ref[S - TIME_BLOCK:S, :]
    tail_ref[TIME_BLOCK:, :] = jnp.zeros((pad, LANES), F32)


def _conv_fwd_call(proj, col0, w, b):
    S = proj.shape[0]
    C = w.shape[1]
    nt = C // LANES
    v0, g0 = col0 // LANES, (col0 + C) // LANES

    def body(val_ref, gate_ref, w_ref, b_ref, o_ref, pad_ref):
        pad_ref[0:CONV_PAD, :] = jnp.zeros((CONV_PAD, LANES), F32)
        pad_ref[CONV_PAD:, :] = val_ref[...] * _sigmoid(gate_ref[...])

        def step(t0):
            o_ref[pl.ds(t0, TIME_BLOCK), :] = _conv_block(pad_ref, t0, w_ref, CONV_WIDTH, CONV_PAD) + b_ref[...]

        _time_loop(S, step)

    seq = lambda off: pl.BlockSpec((S, LANES), lambda i: (0, off + i))
    return pl.pallas_call(
        body, name="conv_module", grid=(nt,),
        in_specs=[seq(v0), seq(g0), pl.BlockSpec((CONV_WIDTH, LANES), lambda i: (0, i)), pl.BlockSpec((1, LANES), lambda i: (0, i))],
        out_specs=seq(0), out_shape=jax.ShapeDtypeStruct((S, C), F32),
        scratch_shapes=[pltpu.VMEM((S + CONV_PAD, LANES), F32)],
        compiler_params=_params(("parallel",)),
    )(proj, proj, w, b)


def _conv_bwd_call(proj, col0, w, dc1):
    S = proj.shape[0]
    C = w.shape[1]
    nt = C // LANES
    v0, g0 = col0 // LANES, (col0 + C) // LANES

    def body(val_ref, gate_ref, w_ref, dy_ref, dval_ref, dgate_ref, dw_ref, db_ref, xpad_ref, tail_ref, dwacc_ref):
        xpad_ref[0:CONV_PAD, :] = jnp.zeros((CONV_PAD, LANES), F32)
        xpad_ref[CONV_PAD:, :] = val_ref[...] * _sigmoid(gate_ref[...])
        _fill_tail(tail_ref, dy_ref, CONV_PAD)
        dwacc_ref[...] = jnp.zeros_like(dwacc_ref)

        def block(t0, dy_src, dy_t0):
            rows = pl.ds(t0, TIME_BLOCK)
            _conv_weight_grad(xpad_ref, t0, dy_ref[rows, :], dwacc_ref, CONV_WIDTH, CONV_PAD)
            dc0 = _conv_transpose_block(dy_src, dy_t0, w_ref, CONV_WIDTH)
            sg = _sigmoid(gate_ref[rows, :])
            dval_ref[rows, :] = (dc0 * sg).astype(BF16)
            dgate_ref[rows, :] = (dc0 * val_ref[rows, :] * sg * (1.0 - sg)).astype(BF16)

        _time_loop(S, lambda t0: block(t0, dy_ref, t0), skip_last=1)
        block(S - TIME_BLOCK, tail_ref, 0)
        dw_ref[...] = dwacc_ref[...]
        db_ref[...] = jnp.sum(dy_ref[...], axis=0, keepdims=True)

    seq = lambda off: pl.BlockSpec((S, LANES), lambda i: (0, off + i))
    return pl.pallas_call(
        body, name="conv_module_bwd", grid=(nt,),
        in_specs=[seq(v0), seq(g0), pl.BlockSpec((CONV_WIDTH, LANES), lambda i: (0, i)), seq(0)],
        out_specs=[seq(0), seq(0), pl.BlockSpec((CONV_PAD, LANES), lambda i: (0, i)), pl.BlockSpec((1, LANES), lambda i: (0, i))],
        out_shape=[jax.ShapeDtypeStruct((S, C), BF16), jax.ShapeDtypeStruct((S, C), BF16),
                   jax.ShapeDtypeStruct((CONV_PAD, C), F32), jax.ShapeDtypeStruct((1, C), F32)],
        scratch_shapes=[pltpu.VMEM((S + CONV_PAD, LANES), F32), pltpu.VMEM((TIME_BLOCK + CONV_PAD, LANES), F32),
                        pltpu.VMEM((CONV_PAD, LANES), F32)],
        compiler_params=_params(("parallel",)),
    )(proj, proj, w, dc1)


def _ffn_fwd_call(u, w, b):
    S, C2 = u.shape
    C = C2 // 2
    nt = C // LANES

    def body(ug_ref, uv_ref, wg_ref, wv_ref, bg_ref, bv_ref, f_ref, hg_ref, hv_ref):
        _fill_head(hg_ref, ug_ref, FFN_PAD)
        _fill_head(hv_ref, uv_ref, FFN_PAD)

        def block(t0, xg_ref, xv_ref, x_t0, pad):
            cg = _conv_block(xg_ref, x_t0, wg_ref, FFN_CONV_WIDTH, pad) + bg_ref[...]
            cv = _conv_block(xv_ref, x_t0, wv_ref, FFN_CONV_WIDTH, pad) + bv_ref[...]
            f_ref[pl.ds(t0, TIME_BLOCK), :] = (_gelu(cg) * cv).astype(BF16)

        block(0, hg_ref, hv_ref, 0, FFN_PAD)
        _time_loop(S, lambda t0: block(t0, ug_ref, uv_ref, t0, 0), skip_first=1)

    seq = lambda off: pl.BlockSpec((S, LANES), lambda i: (0, off + i))
    wsp = lambda off: pl.BlockSpec((FFN_CONV_WIDTH, LANES), lambda i: (0, off + i))
    bsp = lambda off: pl.BlockSpec((1, LANES), lambda i: (0, off + i))
    return pl.pallas_call(
        body, name="ffn_conv_geglu", grid=(nt,),
        in_specs=[seq(0), seq(nt), wsp(0), wsp(nt), bsp(0), bsp(nt)],
        out_specs=seq(0), out_shape=jax.ShapeDtypeStruct((S, C), BF16),
        scratch_shapes=[pltpu.VMEM((FFN_PAD + TIME_BLOCK, LANES), F32)] * 2,
        compiler_params=_params(("parallel",)),
    )(u, u, w, w, b, b)


def _ffn_bwd_call(u, w, b, df):
    S, C2 = u.shape
    C = C2 // 2
    nt = C // LANES

    def body(ug_ref, uv_ref, wg_ref, wv_ref, bg_ref, bv_ref, df_ref,
             du_ref, dwg_ref, dwv_ref, dbg_ref, dbv_ref,
             hg_ref, hv_ref, dg_ref, dv_ref, dwg_acc, dwv_acc, dbg_acc, dbv_acc):
        zeros = jnp.zeros((FFN_PAD, LANES), F32)
        _fill_head(hg_ref, ug_ref, FFN_PAD)
        _fill_head(hv_ref, uv_ref, FFN_PAD)
        dg_ref[S:, :] = zeros
        dv_ref[S:, :] = zeros
        dwg_acc[...] = jnp.zeros_like(dwg_acc)
        dwv_acc[...] = jnp.zeros_like(dwv_acc)
        dbg_acc[...] = jnp.zeros_like(dbg_acc)
        dbv_acc[...] = jnp.zeros_like(dbv_acc)

        def first(t0, xg_ref, xv_ref, x_t0, pad):
            rows = pl.ds(t0, TIME_BLOCK)
            cg = _conv_block(xg_ref, x_t0, wg_ref, FFN_CONV_WIDTH, pad) + bg_ref[...]
            cv = _conv_block(xv_ref, x_t0, wv_ref, FFN_CONV_WIDTH, pad) + bv_ref[...]
            dfb = df_ref[rows, :]
            gelu, gelu_grad = _gelu_and_grad(cg)
            dcg = dfb * cv * gelu_grad
            dcv = dfb * gelu
            dg_ref[rows, :] = dcg
            dv_ref[rows, :] = dcv
            _conv_weight_grad(xg_ref, x_t0, dcg, dwg_acc, FFN_CONV_WIDTH, pad)
            _conv_weight_grad(xv_ref, x_t0, dcv, dwv_acc, FFN_CONV_WIDTH, pad)
            dbg_acc[...] += jnp.sum(dcg, axis=0, keepdims=True)
            dbv_acc[...] += jnp.sum(dcv, axis=0, keepdims=True)

        def second(t0):
            rows = pl.ds(t0, TIME_BLOCK)
            du_ref[0, rows, :] = _conv_transpose_block(dg_ref, t0, wg_ref, FFN_CONV_WIDTH).astype(BF16)
            du_ref[1, rows, :] = _conv_transpose_block(dv_ref, t0, wv_ref, FFN_CONV_WIDTH).astype(BF16)

        first(0, hg_ref, hv_ref, 0, FFN_PAD)
        _time_loop(S, lambda t0: first(t0, ug_ref, uv_ref, t0, 0), skip_first=1)
        _time_loop(S, second)
        dwg_ref[...] = dwg_acc[...]
        dwv_ref[...] = dwv_acc[...]
        dbg_ref[...] = dbg_acc[...]
        dbv_ref[...] = dbv_acc[...]

    seq = lambda off: pl.BlockSpec((S, LANES), lambda i: (0, off + i))
    wsp = lambda off: pl.BlockSpec((FFN_CONV_WIDTH, LANES), lambda i: (0, off + i))
    bsp = lambda off: pl.BlockSpec((1, LANES), lambda i: (0, off + i))
    return pl.pallas_call(
        body, name="ffn_conv_geglu_bwd", grid=(nt,),
        in_specs=[seq(0), seq(nt), wsp(0), wsp(nt), bsp(0), bsp(nt), seq(0)],
        out_specs=[pl.BlockSpec((2, S, LANES), lambda i: (0, 0, i)),
                   pl.BlockSpec((SUBLANES, LANES), lambda i: (0, i)), pl.BlockSpec((SUBLANES, LANES), lambda i: (0, i)),
                   bsp(0), bsp(0)],
        out_shape=[jax.ShapeDtypeStruct((2, S, C), BF16)] + [jax.ShapeDtypeStruct((SUBLANES, C), F32)] * 2
        + [jax.ShapeDtypeStruct((1, C), F32)] * 2,
        scratch_shapes=[pltpu.VMEM((FFN_PAD + TIME_BLOCK, LANES), F32)] * 2 + [pltpu.VMEM((S + FFN_PAD, LANES), F32)] * 2
        + [pltpu.VMEM((SUBLANES, LANES), F32)] * 2
        + [pltpu.VMEM((1, LANES), F32)] * 2,
        compiler_params=_params(("parallel",)),
    )(u, u, w, w, b, b, df)


def _adamw(w_ref, g_ref, m_ref, v_ref, d_ref, mo_ref, vo_ref):
    gv = g_ref[...]
    mn = ADAM_B1 * m_ref[...] + (1.0 - ADAM_B1) * gv
    vn = ADAM_B2 * v_ref[...] + (1.0 - ADAM_B2) * (gv * gv)
    mo_ref[...] = mn
    vo_ref[...] = vn
    m_hat = mn * (1.0 / (1.0 - ADAM_B1 ** ADAM_STEP))
    v_hat = vn * (1.0 / (1.0 - ADAM_B2 ** ADAM_STEP))
    d_ref[...] = -ADAM_LR * (m_hat / (jnp.sqrt(v_hat) + ADAM_EPS) + ADAM_WD * w_ref[...])


def _adamw_call(w, g, m, v, name):
    R, C = w.shape
    tr = _row_tile(R, C)

    def body(w_ref, g_ref, m_ref, v_ref, go_ref, d_ref, mo_ref, vo_ref):
        go_ref[...] = g_ref[...]
        _adamw(w_ref, g_ref, m_ref, v_ref, d_ref, mo_ref, vo_ref)

    spec = pl.BlockSpec((tr, C), lambda i: (i, 0))
    return pl.pallas_call(
        body, name=name, grid=(R // tr,),
        in_specs=[spec] * 4, out_specs=[spec] * 4,
        out_shape=[jax.ShapeDtypeStruct((R, C), F32)] * 4,
        compiler_params=_params(("parallel",)),
    )(w, g, m, v)


def _adamw_small_call(ws, gs, ms, vs):
    n = len(ws)

    def body(*refs):
        w_refs, g_refs, m_refs, v_refs, d_refs, mo_refs, vo_refs = (refs[i * n:(i + 1) * n] for i in range(7))
        for i in range(n):
            _adamw(w_refs[i], g_refs[i], m_refs[i], v_refs[i], d_refs[i], mo_refs[i], vo_refs[i])

    whole = pl.BlockSpec(memory_space=pltpu.VMEM)
    outs = pl.pallas_call(
        body, name="adamw_small",
        in_specs=[whole] * (4 * n), out_specs=[whole] * (3 * n),
        out_shape=[jax.ShapeDtypeStruct(w.shape, F32) for w in ws] * 3,
    )(*ws, *gs, *ms, *vs)
    return outs[:n], outs[n:2 * n], outs[2 * n:]


def _position():
    return lax.axis_index("x"), lax.axis_index("y"), lax.axis_index("c")


def _chip_peers(x, y):
    return [(x, 1 - y), (1 - x, y), (1 - x, 1 - y)]


def _half_rows(ref, core, rows):
    h = rows // 2
    start = pl.multiple_of(core * h, PACKED_ROWS)
    return ref.at[pl.ds(start, h), :] if len(ref.shape) == 2 else ref.at[:, pl.ds(start, h), :]


def _shard_half(ref, shard, core, rows):
    h = rows // 2
    return ref.at[shard, pl.ds(pl.multiple_of(core * h, PACKED_ROWS), h), :]


ANY = pl.BlockSpec(memory_space=pl.ANY)


def _first_hop_copies(srcs, lands):
    x, y, c = _position()
    chip = 2 * x + y
    targets = [(px, py, c) for px, py in _chip_peers(x, y)] + [(x, y, 1 - c)]
    rows = srcs[0].shape[0]
    out = []
    for i, (s, l) in enumerate(zip(srcs, lands)):
        for k, dev in enumerate(targets):
            if i == 0 and k < 3:
                out.append((_half_rows(s, c, rows), _shard_half(l, chip, c, rows), dev, k))
            else:
                out.append((s, l.at[chip], dev, len(targets) * i + k))
    return out


def _second_hop_copies(relations):
    def copies(srcs, lands):
        x, y, c = _position()
        rows = lands[0].shape[1]
        peers = _chip_peers(x, y)
        out = []
        for slot, k in enumerate(relations):
            half = _shard_half(lands[0], 2 * peers[k][0] + peers[k][1], c, rows)
            out.append((half, half, (x, y, 1 - c), slot))
        return out

    return copies


HBM_SPEC = pl.BlockSpec(memory_space=pltpu.HBM)
SEM_SPEC = pl.BlockSpec(memory_space=pltpu.SEMAPHORE)
DATAFLOW = pltpu.SideEffectType.DATAFLOW_SIDE_EFFECTING


def _in_hbm(a):
    return pltpu.with_memory_space_constraint(a, pltpu.HBM)


def _split_start(name, groups, after, carry=None):
    spans, arrays = [], []
    for srcs, lands, _, _ in groups:
        spans.append((len(arrays), len(srcs), len(lands)))
        arrays += list(srcs) + list(lands)
    if carry is not None:
        arrays.append(carry)
    na, ng = len(arrays), len(groups)

    def body(*refs):
        sems, token = refs[na + 1:na + 1 + 2 * ng], refs[-1]
        for g, (_, _, _, copies) in enumerate(groups):
            off, ns, nl = spans[g]
            for src, dst, dev, idx in copies(refs[off:off + ns], refs[off + ns:off + ns + nl]):
                pltpu.make_async_remote_copy(src_ref=src, dst_ref=dst, send_sem=sems[2 * g].at[idx], recv_sem=sems[2 * g + 1].at[idx],
                                             device_id=dev, device_id_type=MESH).start()
        token[...] = jnp.zeros_like(token)

    outs = pl.pallas_call(
        body, name=name,
        in_specs=[HBM_SPEC] * na + [ANY],
        out_specs=[SEM_SPEC] * (2 * ng) + [HBM_SPEC] * na + [pl.BlockSpec(memory_space=pltpu.VMEM)],
        out_shape=[pltpu.SemaphoreType.DMA((n_sems,)) for _, _, n_sems, _ in groups for _ in range(2)]
        + [pltpu.HBM(a.shape, a.dtype) for a in arrays] + [jax.ShapeDtypeStruct((SUBLANES, LANES), F32)],
        input_output_aliases={i: 2 * ng + i for i in range(na)},
        compiler_params=pltpu.CompilerParams(has_side_effects=DATAFLOW),
    )(*[_in_hbm(a) for a in arrays], after)
    started = []
    for g, (off, ns, nl) in enumerate(spans):
        thru = outs[2 * ng + off:2 * ng + off + ns + nl]
        started.append(dict(send=outs[2 * g], recv=outs[2 * g + 1], srcs=list(thru[:ns]), lands=list(thru[ns:]),
                            tile=outs[-1], token=outs[-1][0, 0], carry=None if carry is None else outs[2 * ng + na - 1]))
    return started


def _split_wait(name, started, copies, after, only=None):
    n, m = len(started["srcs"]), len(started["lands"])

    def body(*refs):
        src_refs, land_refs = refs[:n], refs[n:n + m]
        send_sem, recv_sem = refs[n + m], refs[n + m + 1]
        for src, dst, dev, idx in copies(src_refs, land_refs):
            if only is not None and idx not in only:
                continue
            cp = pltpu.make_async_remote_copy(src_ref=src, dst_ref=dst, send_sem=send_sem.at[idx], recv_sem=recv_sem.at[idx],
                                              device_id=dev, device_id_type=MESH)
            cp.wait_send()
            cp.wait_recv()

    arrays = started["srcs"] + started["lands"]
    outs = pl.pallas_call(
        body, name=name,
        in_specs=[HBM_SPEC] * (n + m) + [SEM_SPEC, SEM_SPEC, ANY],
        out_specs=[HBM_SPEC] * (n + m),
        out_shape=[pltpu.HBM(a.shape, a.dtype) for a in arrays],
        input_output_aliases={i: i for i in range(n + m)},
        compiler_params=pltpu.CompilerParams(has_side_effects=DATAFLOW),
    )(*arrays, started["send"], started["recv"], after)
    return list(outs)


def _gather_copies(srcs, lands):
    x, y, c = _position()
    chip = 2 * x + y
    targets = [(px, py, c) for px, py in _chip_peers(x, y)] + [(x, y, 1 - c)]
    return [(s, l.at[chip], dev, len(targets) * i + k) for i, (s, l) in enumerate(zip(srcs, lands)) for k, dev in enumerate(targets)]


def _sibling_copies(srcs, lands):
    x, y, c = _position()
    return [(_half_rows(srcs[0], 1 - c, srcs[0].shape[1]), lands[0], (x, y, 1 - c), 0)]


def _exchange_copies(srcs, lands):
    x, y, c = _position()
    return [(srcs[0].at[2 * px + py], lands[0].at[k], (px, py, c), k) for k, (px, py) in enumerate(_chip_peers(x, y))]


def _pair_sum_call(grad, recv, core, name):
    _, h, B = recv.shape
    tr = _row_tile(h, B)

    def body(core_ref, g_ref, r_ref, o_ref, ob_ref):
        s = g_ref[...] + r_ref[...]
        o_ref[...] = s
        ob_ref[...] = s.astype(BF16)

    g_spec = pl.BlockSpec((None, tr, B), lambda q, i, core_ref: (q, core_ref[0] * (h // tr) + i, 0))
    spec = pl.BlockSpec((None, tr, B), lambda q, i, core_ref: (q, i, 0))
    return pl.pallas_call(
        body, name=name,
        grid_spec=pltpu.PrefetchScalarGridSpec(num_scalar_prefetch=1, grid=(N_CHIPS, h // tr), in_specs=[g_spec, spec],
                                               out_specs=[spec, spec]),
        out_shape=[jax.ShapeDtypeStruct(recv.shape, F32), jax.ShapeDtypeStruct(recv.shape, BF16)],
        compiler_params=_params(("parallel", "parallel")),
    )(core, grad, recv)


def _chip_sum_call(partial, recv, chip_core, name):
    _, h, B = recv.shape
    tr = _row_tile(h, B)

    def body(cc_ref, p_ref, r_ref, o_ref):
        o_ref[...] = ((p_ref[...] + r_ref[0].astype(F32)) + r_ref[1].astype(F32)) + r_ref[2].astype(F32)

    return pl.pallas_call(
        body, name=name,
        grid_spec=pltpu.PrefetchScalarGridSpec(
            num_scalar_prefetch=1, grid=(h // tr,),
            in_specs=[pl.BlockSpec((None, tr, B), lambda i, cc_ref: (cc_ref[0], i, 0)),
                      pl.BlockSpec((3, tr, B), lambda i, cc_ref: (0, i, 0))],
            out_specs=pl.BlockSpec((tr, B), lambda i, cc_ref: (cc_ref[1] * (h // tr) + i, 0))),
        out_shape=jax.ShapeDtypeStruct((2 * h, B), F32),
        compiler_params=_params(("parallel",)),
    )(chip_core, partial, recv)


def _sibling_assemble_call(shards, name="grad_sibling_assemble"):
    n = len(shards)

    def body(*refs):
        ins, outs = refs[:n], refs[n:2 * n]
        send_sems, recv_sems = refs[2 * n:]
        x, y, c = _position()
        copies = []
        for i in range(n):
            rows = shards[i].shape[0]
            cp = pltpu.make_async_remote_copy(src_ref=_half_rows(ins[i], c, rows), dst_ref=_half_rows(outs[i], c, rows),
                                              send_sem=send_sems.at[i], recv_sem=recv_sems.at[i],
                                              device_id=(x, y, 1 - c), device_id_type=MESH)
            cp.start()
            copies.append(cp)
        for cp in copies:
            cp.wait()

    return pl.pallas_call(
        body, name=name,
        in_specs=[ANY] * n, out_specs=[ANY] * n,
        out_shape=[jax.ShapeDtypeStruct(s.shape, F32) for s in shards],
        input_output_aliases={i: i for i in range(n)},
        scratch_shapes=[pltpu.SemaphoreType.DMA((n,)), pltpu.SemaphoreType.DMA((n,))],
    )(*shards)


N_DEVICES = 8


def _allsum_copies(srcs, lands):
    x, y, c = _position()
    me = 4 * x + 2 * y + c
    out = []
    for k in range(1, N_DEVICES):
        peer = (1 - x if k & 4 else x, 1 - y if k & 2 else y, 1 - c if k & 1 else c)
        out.append((srcs[0], lands[0].at[me], peer, k - 1))
    return out


def _ordered_sum_call(mine, landed, me_chip, shapes, sharded_cols):
    rows = mine.shape[0]
    outs = [(s[0], n) if n else s for s, n in zip(shapes, sharded_cols)]

    def body(mc_ref, x_ref, l_ref, *refs):
        acc_ref = refs[-1]
        acc = jnp.where(mc_ref[0] == 0, x_ref[...], l_ref[0])
        for d in range(1, N_DEVICES):
            acc = acc + jnp.where(mc_ref[0] == d, x_ref[...], l_ref[d])
        acc_ref[...] = acc
        first = 0
        for o_ref, (r, c), n in zip(refs[:-1], shapes, sharded_cols):
            per_row = c // LANES

            def unpack(chip, o_ref=o_ref, r=r, n=n, per_row=per_row, first=first):
                for i in range(r):
                    for j in range((n or per_row * LANES) // LANES):
                        src = first + i * per_row + chip * ((n or 0) // LANES) + j
                        o_ref[i:i + 1, j * LANES:(j + 1) * LANES] = acc_ref[src:src + 1, :]

            if n:
                for q in range(N_CHIPS):
                    pl.when(mc_ref[1] == q)(functools.partial(unpack, q))
            else:
                unpack(0)
            first += r * per_row

    results = pl.pallas_call(
        body, name="small_grad_sum",
        in_specs=[pl.BlockSpec(memory_space=pltpu.SMEM), pl.BlockSpec(memory_space=pltpu.VMEM), pl.BlockSpec(memory_space=pltpu.VMEM)],
        out_specs=[pl.BlockSpec(memory_space=pltpu.VMEM)] * len(outs),
        out_shape=[jax.ShapeDtypeStruct(s, F32) for s in outs],
        scratch_shapes=[pltpu.VMEM((rows, LANES), F32)],
    )(me_chip, mine, landed)
    return results


def _pack(arrays):
    flat = jnp.concatenate([a.reshape(-1).astype(F32) for a in arrays])
    rows = -(-flat.shape[0] // LANES)
    rows = -(-rows // SUBLANES) * SUBLANES
    flat = jnp.pad(flat, (0, rows * LANES - flat.shape[0]))
    return flat.reshape(rows, LANES)


def _local_step(xs, target, P, late_weights, on_grad):
    S, D = xs.shape
    qkv_width = 3 * N_HEADS * HEAD_DIM
    glu_col0, gate_col0 = qkv_width, qkv_width + 2 * D
    shard_major = lambda g: g.reshape(N_CHIPS, g.shape[0] // N_CHIPS, g.shape[1])

    h1 = _rms_fwd_call(xs, P["norm_mix_pre"])
    buckets = _bucket_tables()
    bias = _bias_table_call(P["rel_bias"] + 0.0 * h1[0, 0].astype(F32), buckets)
    proj = None
    for stage in ("own", "near", "far"):
        got = late_weights(stage, bias if proj is None else proj)
        proj = _matmul_shards(h1, got.pop("shards"), got.pop("ids"), proj, f"proj_in_{stage}")
    P = dict(P, **got)
    parts = []
    for g in range(N_GROUPS):
        parts += _attn_fwd_call(proj, bias, g)
    a, a_bf, lse = _attn_merge_call(parts)
    P = dict(P, **late_weights("mix", a_bf))
    y_a = _matmul(a_bf, P["w_attn_out"], "nn", "attn_out")
    c1 = _conv_fwd_call(proj, glu_col0, P["conv_dw_w"], P["conv_dw_b"])
    cact = _ln_silu_call(c1, P["conv_ln_g"], P["conv_ln_b"])
    y_c = _matmul(cact, P["conv_pw_w"], "nn", "conv_pw")
    mixed = _mix_call(proj, gate_col0, P["b_gate"], y_a, y_c)
    out = _matmul(mixed, P["w_out"], "nn", "mix_out")
    x1, h2 = _res1_call(xs, out, P["norm_mix_post"], P["norm_ffn_pre"])
    P = dict(P, **late_weights("up", h2))
    u = _matmul(h2, P["w_up"], "nn", "ffn_up")
    f = _ffn_fwd_call(u, P["ffn_conv_w"], P["ffn_conv_b"])
    P = dict(P, **late_weights("down", f))
    yff = _matmul(f, P["w_down"], "nn", "ffn_down")
    loss_tile, dx2, dyff, dg_ffn_post = _loss_call(yff, x1, P["norm_ffn_post"], target)

    G = {}
    G["norm_ffn_post"] = dg_ffn_post
    on_grad("w_down", shard_major(_matmul(f, dyff, "tn", "ffn_down_dw")))
    df = _matmul(dyff, P["w_down"], "nt", "ffn_down_dx")
    du, dwg, dwv, dbg, dbv = _ffn_bwd_call(u, P["ffn_conv_w"], P["ffn_conv_b"], df)
    G["ffn_conv_w"] = jnp.concatenate([dwg[:FFN_CONV_WIDTH], dwv[:FFN_CONV_WIDTH]], axis=1)
    G["ffn_conv_b"] = jnp.concatenate([dbg, dbv], axis=1)
    du = on_grad("w_up", _matmul(h2, du, "tn", "ffn_up_dw", out_shards=True), carry=du)
    dh2 = _matmul(du, P["w_up"], "nt", "ffn_up_dx")
    dx1, dout, G["norm_ffn_pre"], G["norm_mix_post"] = _mid_bwd_call(x1, P["norm_ffn_pre"], dh2, dx2, out, P["norm_mix_post"])
    on_grad("w_out", shard_major(_matmul(mixed, dout, "tn", "mix_out_dw")))
    dmixed = _matmul(dout, P["w_out"], "nt", "mix_out_dx")
    dya, dyc, dga, dgc, dba, dbc = _mix_bwd_call(dmixed, proj, gate_col0, P["b_gate"], y_a, y_c)
    G["b_gate"] = jnp.concatenate([dba, dbc], axis=1)
    on_grad("w_attn_out", _matmul(a_bf, dya, "tn", "attn_out_dw", out_shards=True))
    dyc = on_grad("conv_pw_w", shard_major(_matmul(cact, dyc, "tn", "conv_pw_dw")), carry=dyc)
    da = _matmul(dya, P["w_attn_out"], "nt", "attn_out_dx")
    dcact = _matmul(dyc, P["conv_pw_w"], "nt", "conv_pw_dx")
    dc1, G["conv_ln_g"], G["conv_ln_b"] = _ln_silu_bwd_call(c1, P["conv_ln_g"], P["conv_ln_b"], dcact)
    dval, dgate, dw_dw, G["conv_dw_b"] = _conv_bwd_call(proj, glu_col0, P["conv_dw_w"], dc1)
    G["conv_dw_w"] = dw_dw[:CONV_WIDTH]
    delta = _attn_delta_call(a, da)
    dqs, dks, dvs, dbs = [], [], [], []
    for g in range(N_GROUPS):
        dq, dk, dv, db = _attn_bwd_call(proj, bias, da, lse, delta, g)
        dqs.append(dq)
        dks.append(dk)
        dvs.append(dv)
        dbs.append(db)
    G["rel_bias"] = _bias_grad_call(jnp.concatenate(dbs, axis=0), buckets)
    dproj = _dproj_call(dqs + dks + dvs, [dval, dgate, dga, dgc])
    dproj = on_grad("w_in", _matmul(h1, dproj, "tn", "proj_in_dw", out_shards=True), carry=dproj)
    dh1 = _matmul(dproj, P["w_in"], "nt", "proj_in_dx")
    dh1 = on_grad(None, None, carry=dh1)
    grad_x, G["norm_mix_pre"] = _in_bwd_call(xs, P["norm_mix_pre"], dh1, dx1)
    return loss_tile, grad_x, G


def kernel(x, w_in, b_gate, rel_bias, w_attn_out, conv_dw_w, conv_dw_b, conv_ln_g, conv_ln_b, conv_pw_w, w_out, norm_mix_pre, norm_mix_post, norm_ffn_pre, norm_ffn_post, w_up, ffn_conv_w, ffn_conv_b, w_down, loss_target, m_w_in, m_b_gate, m_rel_bias, m_w_attn_out, m_conv_dw_w, m_conv_dw_b, m_conv_ln_g, m_conv_ln_b, m_conv_pw_w, m_w_out, m_norm_mix_pre, m_norm_mix_post, m_norm_ffn_pre, m_norm_ffn_post, m_w_up, m_ffn_conv_w, m_ffn_conv_b, m_w_down, v_w_in, v_b_gate, v_rel_bias, v_w_attn_out, v_conv_dw_w, v_conv_dw_b, v_conv_ln_g, v_conv_ln_b, v_conv_pw_w, v_w_out, v_norm_mix_pre, v_norm_mix_post, v_norm_ffn_pre, v_norm_ffn_post, v_w_up, v_ffn_conv_w, v_ffn_conv_b, v_w_down):
    weights = dict(w_in=w_in, b_gate=b_gate, rel_bias=rel_bias, w_attn_out=w_attn_out, conv_dw_w=conv_dw_w, conv_dw_b=conv_dw_b,
                   conv_ln_g=conv_ln_g, conv_ln_b=conv_ln_b, conv_pw_w=conv_pw_w, w_out=w_out, norm_mix_pre=norm_mix_pre,
                   norm_mix_post=norm_mix_post, norm_ffn_pre=norm_ffn_pre, norm_ffn_post=norm_ffn_post, w_up=w_up,
                   ffn_conv_w=ffn_conv_w, ffn_conv_b=ffn_conv_b, w_down=w_down)
    m_in = dict(w_in=m_w_in, b_gate=m_b_gate, rel_bias=m_rel_bias, w_attn_out=m_w_attn_out, conv_dw_w=m_conv_dw_w,
                conv_dw_b=m_conv_dw_b, conv_ln_g=m_conv_ln_g, conv_ln_b=m_conv_ln_b, conv_pw_w=m_conv_pw_w, w_out=m_w_out,
                norm_mix_pre=m_norm_mix_pre, norm_mix_post=m_norm_mix_post, norm_ffn_pre=m_norm_ffn_pre,
                norm_ffn_post=m_norm_ffn_post, w_up=m_w_up, ffn_conv_w=m_ffn_conv_w, ffn_conv_b=m_ffn_conv_b, w_down=m_w_down)
    v_in = dict(w_in=v_w_in, b_gate=v_b_gate, rel_bias=v_rel_bias, w_attn_out=v_w_attn_out, conv_dw_w=v_conv_dw_w,
                conv_dw_b=v_conv_dw_b, conv_ln_g=v_conv_ln_g, conv_ln_b=v_conv_ln_b, conv_pw_w=v_conv_pw_w, w_out=v_w_out,
                norm_mix_pre=v_norm_mix_pre, norm_mix_post=v_norm_mix_post, norm_ffn_pre=v_norm_ffn_pre,
                norm_ffn_post=v_norm_ffn_post, w_up=v_w_up, ffn_conv_w=v_ffn_conv_w, ffn_conv_b=v_ffn_conv_b, w_down=v_w_down)
    names = list(weights)
    xi, yi, ci = _position()
    chip = 2 * xi + yi
    core_arr = jnp.reshape(ci, (1,)).astype(jnp.int32)

    xs = x[0]
    target = loss_target[0]
    S, D = xs.shape

    big = ["w_in", "w_attn_out", "conv_pw_w", "w_out", "w_up", "w_down"]
    row_sharded = ("conv_pw_w", "w_out", "w_down")
    natural = lambda k, g: g.reshape(-1, g.shape[2]) if k in row_sharded else g
    first_srcs = [w_in[0].astype(BF16), conv_dw_w[0], ffn_conv_w[0]]
    first_lands = [lax.empty((N_CHIPS,) + s.shape, s.dtype) for s in first_srcs]
    (first_hop,) = _split_start("gather_in_start", [(first_srcs, first_lands, 4 * len(first_srcs), _first_hop_copies)], core_arr)
    launched = first_hop["token"]
    late_sets = dict(mix=["w_attn_out", "conv_pw_w", "w_out"], up=["w_up"], down=["w_down"])
    late_groups = []
    for keys in late_sets.values():
        srcs = [(weights[k][0] + launched).astype(BF16) for k in keys]
        late_groups.append((srcs, [lax.empty((N_CHIPS,) + s.shape, BF16) for s in srcs], 4 * len(keys), _gather_copies))
    started, in_flight = {}, {}
    near, far = (0, 1), (2,)
    far_copy = {2}
    shard_ids = lambda shards: jnp.stack(shards + shards).astype(jnp.int32)

    def late_weights(tag, after):
        n_src = len(first_srcs)
        if tag == "own":
            return dict(shards=first_hop["srcs"][0][None], ids=jnp.stack([0 * chip, chip]).astype(jnp.int32))
        if tag == "near":
            arrays = _split_wait("gather_in_near_wait", first_hop, _first_hop_copies, after,
                                 only=set(range(4 * n_src)) - far_copy)
            (hop,) = _split_start("gather_in_near_pass_start", [([], [arrays[n_src]], len(near), _second_hop_copies(near))], arrays[n_src + 1])
            (w3,) = _split_wait("gather_in_near_pass_wait", hop, _second_hop_copies(near), hop["tile"])
            in_flight.update(dict(first_hop, srcs=arrays[:n_src], lands=[w3] + arrays[n_src + 1:]))
            return dict(shards=w3, ids=shard_ids([jnp.bitwise_xor(chip, 1), jnp.bitwise_xor(chip, 2)]))
        if tag == "far":
            w3, dw4, fc4 = _split_wait("gather_in_far_wait", in_flight, _first_hop_copies, after, only=far_copy)[n_src:]
            hop, *late = _split_start("gather_in_far_pass_start", [([], [w3], len(far), _second_hop_copies(far))] + late_groups, dw4)
            started.update(zip(late_sets, late))
            (w_in_full,) = _split_wait("gather_in_far_pass_wait", hop, _second_hop_copies(far), hop["tile"])
            return dict(shards=w_in_full, ids=shard_ids([jnp.bitwise_xor(chip, 3)]), w_in=w_in_full,
                        conv_dw_w=jnp.concatenate(list(dw4), axis=1), ffn_conv_w=jnp.concatenate(list(fc4), axis=1))
        landed = _split_wait(f"gather_{tag}_wait", started[tag], _gather_copies, after)[len(late_sets[tag]):]
        return {k: natural(k, g) for k, g in zip(late_sets[tag], landed)}

    chip_core = jnp.stack([chip, ci]).astype(jnp.int32)
    exchanging, pending = {}, {}

    held = []

    def launch(tag, after, carry=None):
        keys, groups, partial = [], [], {}
        for k in list(exchanging):
            gk, r1 = _split_wait(f"sibling_exchange_wait_{k}", exchanging.pop(k), _sibling_copies, after)
            partial[k], s16 = _pair_sum_call(gk, r1, core_arr, f"pair_sum_{k}")
            keys.append(k)
            groups.append(([s16], [lax.empty((3,) + s16.shape[1:], BF16)], 3, _exchange_copies))
        fresh = [k for k, _ in held]
        for _, g3 in held:
            groups.append(([g3], [lax.empty((N_CHIPS, g3.shape[1] // 2, g3.shape[2]), F32)], 1, _sibling_copies))
        held.clear()
        begun = _split_start(f"grad_exchange_start_{tag}", groups, core_arr, carry)
        for k, st in zip(keys + fresh, begun):
            if k in partial:
                pending[k] = (partial[k], st)
            else:
                exchanging[k] = st
        return begun[0]["carry"]

    def on_grad(k, g3, carry=None):
        if k is None:
            return launch("last", carry[:SUBLANES, :LANES], carry)
        held.append((k, g3))
        if k in ("w_down", "w_out", "w_attn_out"):
            return carry
        return launch(k, g3[0, :SUBLANES, :LANES], carry)

    def finish(keys, after, tag):
        halves = []
        for k in keys:
            s32, st = pending[k]
            recv2 = _split_wait(f"chip_exchange_wait_{k}", st, _exchange_copies, after)[1]
            halves.append(_chip_sum_call(s32, recv2, chip_core, f"chip_sum_{k}"))
        return dict(zip(keys, _sibling_assemble_call(halves, f"grad_sibling_assemble_{tag}")))

    P = dict(b_gate=b_gate, rel_bias=rel_bias, conv_dw_b=conv_dw_b, conv_ln_g=conv_ln_g, conv_ln_b=conv_ln_b,
             norm_mix_pre=norm_mix_pre + launched, norm_mix_post=norm_mix_post, norm_ffn_pre=norm_ffn_pre,
             norm_ffn_post=norm_ffn_post, ffn_conv_b=ffn_conv_b)
    loss_tile, grad_x, G = _local_step(xs, target, P, late_weights, on_grad)

    small = [k for k in names if k not in big]
    packed = _pack([loss_tile[:1]] + [G[k] for k in small])
    (allsum,) = _split_start("small_grad_allsum_start",
                             [([packed], [jnp.zeros((N_DEVICES,) + packed.shape, F32)], N_DEVICES - 1, _allsum_copies)], core_arr)

    reduced, grads, deltas, new_m, new_v = {}, {}, {}, {}, {}

    def update(keys):
        for k in keys:
            gk, d, mn, vn = _adamw_call(weights[k][0], reduced[k], m_in[k][0], v_in[k][0], f"adamw_{k}")
            grads[k], deltas[k], new_m[k], new_v[k] = gk[None], d[None], mn[None], vn[None]

    others = [k for k in big if k != "w_in"]
    reduced.update(finish(others, allsum["tile"], "others"))
    update(others)
    reduced.update(finish(["w_in"], deltas["w_up"], "w_in"))
    update(["w_in"])

    me_chip = jnp.stack([4 * xi + 2 * yi + ci, chip]).astype(jnp.int32)
    mine, landed = _split_wait("small_grad_allsum_wait", allsum, _allsum_copies, deltas["w_in"])
    piece_shapes = [(1, LANES)] + [(G[k].size // LANES, LANES) if k == "rel_bias" else G[k].shape for k in small]
    piece_cols = [0] + [weights[k].shape[2] if k in ("conv_dw_w", "ffn_conv_w") else 0 for k in small]
    loss_row, *summed = _ordered_sum_call(mine, landed, me_chip, piece_shapes, piece_cols)
    loss = loss_row[0, 0]
    for k, gsum in zip(small, summed):
        grads[k] = gsum.reshape(weights[k].shape)
    ds, mns, vns = _adamw_small_call([weights[k] for k in small], [grads[k] for k in small],
                                     [m_in[k] for k in small], [v_in[k] for k in small])
    deltas.update(zip(small, ds))
    new_m.update(zip(small, mns))
    new_v.update(zip(small, vns))

    return (loss, grad_x[None], *[grads[k] for k in names], *[deltas[k] for k in names],
            *[new_m[k] for k in names], *[new_v[k] for k in names])
```

```python
import functools
import math

import jax
import jax.numpy as jnp
import numpy as np
from jax import lax
from jax.experimental import pallas as pl
from jax.experimental.pallas import tpu as pltpu

F32 = jnp.float32
BF16 = jnp.bfloat16
MESH = pl.DeviceIdType.MESH

HEAD_DIM = 128
HEADS_PER_GROUP = 4
DILATED_PATTERNS = ((128, 1), (512, 4), (2048, 16))
N_GROUPS = 3
N_HEADS = N_GROUPS * HEADS_PER_GROUP
SPAN = 128
GROUP_WIDTH = HEADS_PER_GROUP * HEAD_DIM
CONV_WIDTH = 31
FFN_CONV_WIDTH = 3
N_BUCKETS = 32
MAX_DISTANCE = 2048
RMS_EPS = 1e-6
LN_EPS = 1e-5
NEG_INF = -1e30
ADAM_LR = 0.001
ADAM_B1 = 0.9
ADAM_B2 = 0.999
ADAM_EPS = 1e-08
ADAM_WD = 0.01
ADAM_STEP = 10

LANES = 128
SUBLANES = 8
PACKED_ROWS = 16
ROW_TILE = 512
GATE_ROWS, GATE_COLS = 512, 512
TIME_BLOCK = 128
CONV_PAD = 32
FFN_PAD = 8
VMEM_LIMIT = 56 << 20


def _params(sem=None, vmem=None):
    kw = {}
    if sem is not None:
        kw["dimension_semantics"] = sem
    if vmem is not None:
        kw["vmem_limit_bytes"] = vmem
    return pltpu.CompilerParams(**kw)


def _pick(n, cands):
    for c in cands:
        if n % c == 0:
            return c
    return n


ELEMENTWISE_TILE_BYTES = 3 << 19


def _row_tile(rows, cols):
    for align in (16, SUBLANES):
        fits = [t for t in range(align, rows + 1, align) if rows % t == 0 and t * cols * 4 <= ELEMENTWISE_TILE_BYTES]
        if fits:
            return max(fits)
    return SUBLANES


N_CHIPS = 4
M_TILES = (1024, 1408, 512, 256, 128)
N_TILES = (1024, 512, 1408, 256, 128)
K_TILES = (2176, 2048, 1408, 1024, 512, 256, 128)


def _matmul(a, b, mode, name, out_shards=False, tm=None):
    assert a.dtype == BF16 and b.dtype == BF16, (name, a.dtype, b.dtype)
    b3 = b.ndim == 3
    tn = tk = None
    halves = None
    if mode == "nn":
        M, K = a.shape
        N = b.shape[-1] * (N_CHIPS if b3 else 1)
        tn = b.shape[-1] if b3 else None
    elif mode == "nt":
        if a.ndim == 3:
            halves = a.shape[2]
        M, K = a.shape[-2], a.shape[-1] * (a.shape[0] if a.ndim == 3 else 1)
        N = b.shape[-2]
        tk = b.shape[-1] if b3 else None
    else:
        if b3:
            halves = b.shape[2]
        K, M = a.shape
        N = b.shape[-1] * (b.shape[0] if b3 else 1)
        tn = N // N_CHIPS if out_shards else None
    tm = tm or _pick(M, M_TILES)
    tn = tn or _pick(N, N_TILES)
    tk = tk or _pick(K, K_TILES)
    nk = K // tk
    dn = {"nn": (((1,), (0,)), ((), ())), "nt": (((1,), (1,)), ((), ())), "tn": (((0,), (0,)), ((), ()))}[mode]

    def body(a_ref, b_ref, o_ref):
        if nk == 1:
            o_ref[...] = lax.dot_general(a_ref[...], b_ref[...], dn, preferred_element_type=F32)
        else:
            @pl.when(pl.program_id(2) == 0)
            def _():
                o_ref[...] = jnp.zeros_like(o_ref)

            o_ref[...] += lax.dot_general(a_ref[...], b_ref[...], dn, preferred_element_type=F32)

    if mode == "tn":
        a_spec = pl.BlockSpec((tk, tm), lambda i, j, k: (k, i))
    elif halves:
        per = halves // tk
        a_spec = pl.BlockSpec((None, tm, tk), lambda i, j, k: (k // per, i, k % per))
    else:
        a_spec = pl.BlockSpec((tm, tk), lambda i, j, k: (i, k))
    if mode == "nn":
        b_spec = pl.BlockSpec((None, tk, tn), lambda i, j, k: (j, k, 0)) if b3 else pl.BlockSpec((tk, tn), lambda i, j, k: (k, j))
    elif mode == "nt":
        b_spec = pl.BlockSpec((None, tn, tk), lambda i, j, k: (k, j, 0)) if b3 else pl.BlockSpec((tn, tk), lambda i, j, k: (j, k))
    elif halves:
        per = halves // tn
        b_spec = pl.BlockSpec((None, tk, tn), lambda i, j, k: (j // per, k, j % per))
    else:
        b_spec = pl.BlockSpec((tk, tn), lambda i, j, k: (k, j))
    if out_shards:
        out_spec = pl.BlockSpec((None, tm, tn), lambda i, j, k: (j, i, 0))
        out_shape = jax.ShapeDtypeStruct((N_CHIPS, M, tn), F32)
    else:
        out_spec = pl.BlockSpec((tm, tn), lambda i, j, k: (i, j))
        out_shape = jax.ShapeDtypeStruct((M, N), F32)
    return pl.pallas_call(
        body, name=name, grid=(M // tm, N // tn, nk),
        in_specs=[a_spec, b_spec], out_specs=out_spec, out_shape=out_shape,
        compiler_params=_params(("parallel", "parallel", "arbitrary"), VMEM_LIMIT),
    )(a, b)


def _grad_half_matmul(a, name, half, b, init=None):
    K, M = a.shape
    parts = b.ndim == 3
    N = b.shape[-1] * (b.shape[0] if parts else 1)
    h, tn = M // 2, N // N_CHIPS
    dn = (((0,), (0,)), ((), ()))

    def body(half_ref, a_ref, b_ref, *rest):
        product = lax.dot_general(a_ref[...], b_ref[...], dn, preferred_element_type=F32)
        if init is None:
            rest[0][...] = product
        else:
            total = product + rest[0][...]
            rest[1][...] = total
            rest[2][...] = total.astype(BF16)

    if parts:
        per = b.shape[2] // tn
        b_spec = pl.BlockSpec((None, K, tn), lambda j, half_ref: (j // per, 0, j % per))
    else:
        b_spec = pl.BlockSpec((K, tn), lambda j, half_ref: (0, j))
    out_spec = pl.BlockSpec((None, h, tn), lambda j, half_ref: (j, 0, 0))
    shape = (N_CHIPS, h, tn)
    summed = init is not None
    return pl.pallas_call(
        body, name=name + ("_mine" if summed else "_theirs"),
        grid_spec=pltpu.PrefetchScalarGridSpec(
            num_scalar_prefetch=1, grid=(N_CHIPS,),
            in_specs=[pl.BlockSpec((K, h), lambda j, half_ref: (0, half_ref[0])), b_spec] + [out_spec] * summed,
            out_specs=[out_spec, out_spec] if summed else out_spec),
        out_shape=[jax.ShapeDtypeStruct(shape, F32), jax.ShapeDtypeStruct(shape, BF16)] if summed
        else jax.ShapeDtypeStruct(shape, F32),
        compiler_params=_params(("parallel",), VMEM_LIMIT),
    )(half, a, b, *([init] if summed else []))


def _rms(x, g):
    r = lax.rsqrt(jnp.mean(x * x, axis=-1, keepdims=True) + RMS_EPS)
    return x * r * g


def _rms_bwd(x, g, dy):
    r = lax.rsqrt(jnp.mean(x * x, axis=-1, keepdims=True) + RMS_EPS)
    n = x * r
    dn = dy * g
    dx = r * (dn - n * jnp.mean(dn * n, axis=-1, keepdims=True))
    return dx, jnp.sum(dy * n, axis=0, keepdims=True)


def _sigmoid(x):
    return 1.0 / (1.0 + jnp.exp(-x))


_GELU_C = math.sqrt(2.0 / math.pi)


def _gelu(x):
    return 0.5 * x * (1.0 + jnp.tanh(_GELU_C * (x + 0.044715 * x * x * x)))


def _gelu_and_grad(x):
    x2 = x * x
    t = jnp.tanh(_GELU_C * x * (1.0 + 0.044715 * x2))
    half = 0.5 * (1.0 + t)
    return x * half, half + (0.5 * _GELU_C) * x * (1.0 - t * t) * (1.0 + (3.0 * 0.044715) * x2)


def _row_spec(width, col_block=0):
    return pl.BlockSpec((ROW_TILE, width), lambda i: (i, col_block))


def _vec_spec(width, col_block=0):
    return pl.BlockSpec((1, width), lambda i: (0, col_block))


def _accumulate(ref, part):
    @pl.when(pl.program_id(0) == 0)
    def _():
        ref[...] = part

    @pl.when(pl.program_id(0) > 0)
    def _():
        ref[...] += part


def _rms_fwd_call(x, g):
    S, D = x.shape

    def body(x_ref, g_ref, h_ref):
        h_ref[...] = _rms(x_ref[...], g_ref[...]).astype(BF16)

    return pl.pallas_call(
        body, name="rms_mix_pre", grid=(S // ROW_TILE,),
        in_specs=[_row_spec(D), _vec_spec(D)], out_specs=_row_spec(D),
        out_shape=jax.ShapeDtypeStruct((S, D), BF16),
        compiler_params=_params(("parallel",)),
    )(x, g)


def _ln_silu_call(c1, g, b):
    S, C = c1.shape

    def body(c_ref, g_ref, b_ref, o_ref):
        xv = c_ref[...]
        mu = jnp.mean(xv, axis=-1, keepdims=True)
        xc = xv - mu
        var = jnp.mean(xc * xc, axis=-1, keepdims=True)
        z = xc * lax.rsqrt(var + LN_EPS) * g_ref[...] + b_ref[...]
        o_ref[...] = (z * _sigmoid(z)).astype(BF16)

    return pl.pallas_call(
        body, name="conv_ln_silu", grid=(S // ROW_TILE,),
        in_specs=[_row_spec(C), _vec_spec(C), _vec_spec(C)], out_specs=_row_spec(C),
        out_shape=jax.ShapeDtypeStruct((S, C), BF16),
        compiler_params=_params(("parallel",)),
    )(c1, g, b)


def _ln_silu_bwd_call(c1, g, b, dc):
    S, C = c1.shape

    def body(c_ref, g_ref, b_ref, dc_ref, dx_ref, dg_ref, db_ref):
        xv = c_ref[...]
        mu = jnp.mean(xv, axis=-1, keepdims=True)
        xc = xv - mu
        rs = lax.rsqrt(jnp.mean(xc * xc, axis=-1, keepdims=True) + LN_EPS)
        xh = xc * rs
        z = xh * g_ref[...] + b_ref[...]
        sg = _sigmoid(z)
        dz = dc_ref[...] * (sg * (1.0 + z * (1.0 - sg)))
        dxh = dz * g_ref[...]
        dx_ref[...] = rs * (dxh - jnp.mean(dxh, axis=-1, keepdims=True) - xh * jnp.mean(dxh * xh, axis=-1, keepdims=True))
        _accumulate(dg_ref, jnp.sum(dz * xh, axis=0, keepdims=True))
        _accumulate(db_ref, jnp.sum(dz, axis=0, keepdims=True))

    return pl.pallas_call(
        body, name="conv_ln_silu_bwd", grid=(S // ROW_TILE,),
        in_specs=[_row_spec(C), _vec_spec(C), _vec_spec(C), _row_spec(C)],
        out_specs=[_row_spec(C), _vec_spec(C), _vec_spec(C)],
        out_shape=[jax.ShapeDtypeStruct((S, C), F32), jax.ShapeDtypeStruct((1, C), F32), jax.ShapeDtypeStruct((1, C), F32)],
        compiler_params=_params(("arbitrary",)),
    )(c1, g, b, dc)


def _mix_call(proj, gate_col0, b_gate, y_a, y_c):
    S, D = y_a.shape
    w = GATE_COLS
    nc = D // w
    ga0, gc0 = gate_col0 // w, (gate_col0 + D) // w

    def body(ga_ref, gc_ref, ba_ref, bc_ref, ya_ref, yc_ref, o_ref):
        o_ref[...] = (_sigmoid(ga_ref[...] + ba_ref[...]) * ya_ref[...]
                      + _sigmoid(gc_ref[...] + bc_ref[...]) * yc_ref[...]).astype(BF16)

    tile = lambda off: pl.BlockSpec((GATE_ROWS, w), lambda i, j: (i, off + j))
    vec = lambda off: pl.BlockSpec((1, w), lambda i, j: (0, off + j))
    return pl.pallas_call(
        body, name="gate_mix", grid=(S // GATE_ROWS, nc),
        in_specs=[tile(ga0), tile(gc0), vec(0), vec(nc), tile(0), tile(0)],
        out_specs=tile(0), out_shape=jax.ShapeDtypeStruct((S, D), BF16),
        compiler_params=_params(("parallel", "parallel")),
    )(proj, proj, b_gate, b_gate, y_a, y_c)


def _mix_bwd_call(dmixed, proj, gate_col0, b_gate, y_a, y_c):
    S, D = y_a.shape
    w = GATE_COLS
    nc = D // w
    ga0, gc0 = gate_col0 // w, (gate_col0 + D) // w

    def body(dm_ref, ga_ref, gc_ref, ba_ref, bc_ref, ya_ref, yc_ref, dya_ref, dyc_ref, dga_ref, dgc_ref, dba_ref, dbc_ref):
        dm = dm_ref[...]
        sa = _sigmoid(ga_ref[...] + ba_ref[...])
        sc = _sigmoid(gc_ref[...] + bc_ref[...])
        dya_ref[...] = (dm * sa).astype(BF16)
        dyc_ref[...] = (dm * sc).astype(BF16)
        dga = dm * ya_ref[...] * sa * (1.0 - sa)
        dgc = dm * yc_ref[...] * sc * (1.0 - sc)
        dga_ref[...] = dga.astype(BF16)
        dgc_ref[...] = dgc.astype(BF16)
        pa = jnp.sum(dga, axis=0, keepdims=True)
        pc = jnp.sum(dgc, axis=0, keepdims=True)

        @pl.when(pl.program_id(1) == 0)
        def _():
            dba_ref[...] = pa
            dbc_ref[...] = pc

        @pl.when(pl.program_id(1) > 0)
        def _():
            dba_ref[...] += pa
            dbc_ref[...] += pc

    tile = lambda off: pl.BlockSpec((GATE_ROWS, w), lambda j, i: (i, off + j))
    vec = lambda off: pl.BlockSpec((1, w), lambda j, i: (0, off + j))
    return pl.pallas_call(
        body, name="gate_mix_bwd", grid=(nc, S // GATE_ROWS),
        in_specs=[tile(0), tile(ga0), tile(gc0), vec(0), vec(nc), tile(0), tile(0)],
        out_specs=[tile(0), tile(0), tile(0), tile(0), vec(0), vec(0)],
        out_shape=[jax.ShapeDtypeStruct((S, D), BF16)] * 4 + [
                   jax.ShapeDtypeStruct((1, D), F32), jax.ShapeDtypeStruct((1, D), F32)],
        compiler_params=_params(("parallel", "arbitrary")),
    )(dmixed, proj, proj, b_gate, b_gate, y_a, y_c)


def _res1_call(x, out, g_post, g_pre):
    S, D = x.shape

    def body(x_ref, o_ref, gp_ref, gq_ref, x1_ref, h2_ref):
        x1 = x_ref[...] + _rms(o_ref[...], gp_ref[...])
        x1_ref[...] = x1
        h2_ref[...] = _rms(x1, gq_ref[...]).astype(BF16)

    return pl.pallas_call(
        body, name="residual_mix", grid=(S // ROW_TILE,),
        in_specs=[_row_spec(D), _row_spec(D), _vec_spec(D), _vec_spec(D)],
        out_specs=[_row_spec(D), _row_spec(D)],
        out_shape=[jax.ShapeDtypeStruct((S, D), F32), jax.ShapeDtypeStruct((S, D), BF16)],
        compiler_params=_params(("parallel",)),
    )(x, out, g_post, g_pre)


def _loss_call(y, x1, g_post, target):
    S, D = y.shape

    def body(y_ref, x1_ref, g_ref, t_ref, loss_ref, dx_ref, dy_ref, dg_ref):
        yv, gv = y_ref[...], g_ref[...]
        err = x1_ref[...] + _rms(yv, gv) - t_ref[...]
        dx2 = err * (1.0 / D)
        dx_ref[...] = dx2
        dy, dg = _rms_bwd(yv, gv, dx2)
        dy_ref[...] = dy.astype(BF16)
        _accumulate(dg_ref, dg)
        part = 0.5 * jnp.sum(jnp.mean(err * err, axis=-1, keepdims=True), axis=0, keepdims=True)
        _accumulate(loss_ref, jnp.broadcast_to(part, (SUBLANES, LANES)))

    return pl.pallas_call(
        body, name="residual_ffn_loss", grid=(S // ROW_TILE,),
        in_specs=[_row_spec(D), _row_spec(D), _vec_spec(D), _row_spec(D)],
        out_specs=[pl.BlockSpec((SUBLANES, LANES), lambda i: (0, 0)), _row_spec(D), _row_spec(D), _vec_spec(D)],
        out_shape=[jax.ShapeDtypeStruct((SUBLANES, LANES), F32), jax.ShapeDtypeStruct((S, D), F32),
                   jax.ShapeDtypeStruct((S, D), BF16), jax.ShapeDtypeStruct((1, D), F32)],
        compiler_params=_params(("arbitrary",)),
    )(y, x1, g_post, target)


def _mid_bwd_call(x1, g_pre, dh2, dx2, out, g_post):
    S, D = x1.shape

    def body(x1_ref, gq_ref, dh_ref, dx2_ref, o_ref, gp_ref, dx1_ref, do_ref, dgq_ref, dgp_ref):
        d, dgq = _rms_bwd(x1_ref[...], gq_ref[...], dh_ref[...])
        dx1 = dx2_ref[...] + d
        dx1_ref[...] = dx1
        do, dgp = _rms_bwd(o_ref[...], gp_ref[...], dx1)
        do_ref[...] = do.astype(BF16)
        _accumulate(dgq_ref, dgq)
        _accumulate(dgp_ref, dgp)

    return pl.pallas_call(
        body, name="residual_mix_bwd", grid=(S // ROW_TILE,),
        in_specs=[_row_spec(D), _vec_spec(D), _row_spec(D), _row_spec(D), _row_spec(D), _vec_spec(D)],
        out_specs=[_row_spec(D), _row_spec(D), _vec_spec(D), _vec_spec(D)],
        out_shape=[jax.ShapeDtypeStruct((S, D), F32), jax.ShapeDtypeStruct((S, D), BF16)] + [jax.ShapeDtypeStruct((1, D), F32)] * 2,
        compiler_params=_params(("arbitrary",)),
    )(x1, g_pre, dh2, dx2, out, g_post)


def _in_bwd_call(x, g, dh1, dx1):
    S, D = x.shape

    def body(x_ref, g_ref, dh_ref, dx1_ref, gx_ref, dg_ref):
        d, dg = _rms_bwd(x_ref[...], g_ref[...], dh_ref[...])
        gx_ref[...] = dx1_ref[...] + d
        _accumulate(dg_ref, dg)

    return pl.pallas_call(
        body, name="rms_mix_pre_bwd", grid=(S // ROW_TILE,),
        in_specs=[_row_spec(D), _vec_spec(D), _row_spec(D), _row_spec(D)],
        out_specs=[_row_spec(D), _vec_spec(D)],
        out_shape=[jax.ShapeDtypeStruct((S, D), F32), jax.ShapeDtypeStruct((1, D), F32)],
        compiler_params=_params(("arbitrary",)),
    )(x, g, dh1, dx1)


def _bucket_table(dilation):
    qi = np.arange(SPAN)[:, None]
    ki = np.arange(2 * SPAN)[None, :]
    dist = np.maximum(qi + SPAN - ki, 0) * dilation
    max_exact = N_BUCKETS // 2
    d = np.maximum(dist, 1).astype(np.float64)
    large = max_exact + (np.log(d / max_exact) / math.log(MAX_DISTANCE / max_exact) * (N_BUCKETS - max_exact)).astype(np.int32)
    large = np.minimum(large, N_BUCKETS - 1)
    return np.where(dist < max_exact, dist, large).astype(np.int32)


def _bucket_tables():
    return jnp.asarray(np.stack([_bucket_table(r) for _, r in DILATED_PATTERNS]))


def _bias_table_call(rel_bias, buckets):
    def body(rb_ref, bk_ref, o_ref):
        for h in range(N_HEADS):
            bk = bk_ref[h // HEADS_PER_GROUP]

            def step(b, acc):
                return jnp.where(bk == b, rb_ref[b, h], acc)

            o_ref[h] = lax.fori_loop(0, N_BUCKETS, step, jnp.zeros((SPAN, 2 * SPAN), F32))

    return pl.pallas_call(
        body, name="rel_bias_table",
        in_specs=[pl.BlockSpec(memory_space=pltpu.SMEM), pl.BlockSpec(memory_space=pltpu.VMEM)],
        out_specs=pl.BlockSpec(memory_space=pltpu.VMEM),
        out_shape=jax.ShapeDtypeStruct((N_HEADS, SPAN, 2 * SPAN), F32),
    )(rel_bias, buckets)


def _bias_grad_call(dbias, buckets):
    def body(db_ref, bk_ref, o_ref, rows_ref):
        for h in range(N_HEADS):
            bk = bk_ref[h // HEADS_PER_GROUP]
            dv = db_ref[h]

            def step(b, carry):
                rows_ref[h, b] = jnp.sum(jnp.where(bk == b, dv, 0.0), axis=0, keepdims=True)
                return carry

            lax.fori_loop(0, N_BUCKETS, step, 0)
        o_ref[...] = jnp.sum(rows_ref[...], axis=-1, keepdims=True)

    out = pl.pallas_call(
        body, name="rel_bias_grad",
        in_specs=[pl.BlockSpec(memory_space=pltpu.VMEM), pl.BlockSpec(memory_space=pltpu.VMEM)],
        out_specs=pl.BlockSpec(memory_space=pltpu.VMEM),
        out_shape=jax.ShapeDtypeStruct((N_HEADS, N_BUCKETS, 1, 1), F32),
        scratch_shapes=[pltpu.VMEM((N_HEADS, N_BUCKETS, 1, 2 * SPAN), F32)],
    )(dbias, buckets)
    return out.reshape(N_HEADS, N_BUCKETS).T


def _dot_nt(a, b):
    return lax.dot_general(a, b, (((1,), (1,)), ((), ())), preferred_element_type=F32)


def _dot_nn(a, b):
    return lax.dot_general(a, b, (((1,), (0,)), ((), ())), preferred_element_type=F32)


def _dot_tn(a, b):
    return lax.dot_general(a, b, (((0,), (0,)), ((), ())), preferred_element_type=F32)


def _band_masks(n, nb):
    qi = lax.broadcasted_iota(jnp.int32, (SPAN, SPAN), 0)
    ki = lax.broadcasted_iota(jnp.int32, (SPAN, SPAN), 1)
    prev_ok = jnp.logical_and(ki >= qi, n > 0)
    cur_ok = ki <= qi
    next_ok = jnp.logical_and(ki >= qi, n < nb - 1)
    return prev_ok, cur_ok, next_ok


def _wide_band_mask(n):
    qi = lax.broadcasted_iota(jnp.int32, (SPAN, 2 * SPAN), 0)
    ki = lax.broadcasted_iota(jnp.int32, (SPAN, 2 * SPAN), 1)
    prev_ok = jnp.logical_and(jnp.logical_and(ki < SPAN, ki >= qi), n > 0)
    cur_ok = jnp.logical_and(ki >= SPAN, ki - SPAN <= qi)
    return jnp.logical_or(prev_ok, cur_ok)


def _attn_plan(S, group):
    r = DILATED_PATTERNS[group][1]
    hp, per = (HEADS_PER_GROUP, 1) if r == 1 else (2, 4)
    return r, S // (r * SPAN), hp, per


def _residue_rows(rho, r):
    return slice(None) if r == 1 else pl.ds(rho, SPAN, stride=r)


def _for_residues(r, per, fn):
    if r == per:
        for u in range(per):
            fn(u)
        return

    def step(i, carry):
        for u in range(per):
            fn(i * per + u)
        return carry

    lax.fori_loop(0, r // per, step, 0)


def _attn_fwd_call(proj, bias, group):
    S = proj.shape[0]
    r, nb, hp, per = _attn_plan(S, group)
    scale = HEAD_DIM ** -0.5
    kinds = ("q", "kp", "kc", "vp", "vc") if nb > 1 else ("q", "kc", "vc")

    per_kind = _refs_per_kind(r, hp)

    def body(*refs):
        ins = {kind: refs[i * per_kind:(i + 1) * per_kind] for i, kind in enumerate(kinds)}
        b_ref, o_ref, lse_ref = refs[len(kinds) * per_kind:]
        n = pl.program_id(1)
        prev_ok, cur_ok, _ = _band_masks(n, nb)

        band_ok = _wide_band_mask(n) if nb > 1 else cur_ok

        def residue(rho):
            rows = _residue_rows(rho, r)
            for j in range(hp):
                get = lambda kind: _head_rows(ins[kind], j, rows, r).astype(BF16)
                q = get("q")
                if nb > 1:
                    keys, vals, bias_j = jnp.concatenate([get("kp"), get("kc")], axis=0), jnp.concatenate([get("vp"), get("vc")], axis=0), b_ref[j]
                else:
                    keys, vals, bias_j = get("kc"), get("vc"), b_ref[j, :, SPAN:]
                s = jnp.where(band_ok, _dot_nt(q, keys) * scale + bias_j, NEG_INF)
                m = jnp.max(s, axis=-1, keepdims=True)
                p = jnp.exp(s - m)
                den = jnp.sum(p, axis=-1, keepdims=True)
                o_ref[j, rows, :] = _dot_nn(p.astype(BF16), vals) / den
                lse_ref[j, rows, :] = jnp.broadcast_to(m + jnp.log(den), (SPAN, HEAD_DIM))

        _for_residues(r, per, residue)

    in_specs = [_head_spec(r, nb, hp, kind, group, jj) for kind in kinds for jj in range(per_kind)]
    in_specs.append(pl.BlockSpec((hp, SPAN, 2 * SPAN), lambda j, n: (group * (HEADS_PER_GROUP // hp) + j, 0, 0)))
    out = pl.BlockSpec((hp, r * SPAN, HEAD_DIM), lambda j, n: (j, n, 0))
    return pl.pallas_call(
        body, name=f"attn_fwd_g{group}", grid=(HEADS_PER_GROUP // hp, nb),
        in_specs=in_specs, out_specs=[out] * 2,
        out_shape=[jax.ShapeDtypeStruct((HEADS_PER_GROUP, S, HEAD_DIM), F32)] * 2,
        compiler_params=_params(("parallel", "parallel"), VMEM_LIMIT),
    )(*([proj] * (len(in_specs) - 1)), bias)


_PROJ_PART = dict(q=0, qn=0, kp=1, kc=1, vp=2, vc=2)


def _refs_per_kind(r, hp):
    return 1 if r == 1 else hp


def _head_rows(refs, j, rows, r):
    return refs[0][:, j * HEAD_DIM:(j + 1) * HEAD_DIM] if r == 1 else refs[j][rows, :]


def _head_spec(r, nb, hp, kind, group, jj):
    if kind in _PROJ_PART:
        base = (_PROJ_PART[kind] * N_GROUPS + group) * HEADS_PER_GROUP
    else:
        base = 0
    if kind.endswith("p"):
        row = lambda n: jnp.maximum(n - 1, 0)
    elif kind.endswith("n"):
        row = lambda n: jnp.minimum(n + 1, nb - 1)
    else:
        row = lambda n: n
    if r == 1:
        return pl.BlockSpec((SPAN, hp * HEAD_DIM), lambda j, n: (row(n), base // hp + j))
    return pl.BlockSpec((r * SPAN, HEAD_DIM), lambda j, n: (row(n), base + j * hp + jj))


def _attn_merge_call(parts):
    S = parts[0].shape[1]

    def body(o1, s1, o2, s2, o3, s3, a_ref, ab_ref, lse_ref):
        for j in range(HEADS_PER_GROUP):
            sl = slice(j * HEAD_DIM, (j + 1) * HEAD_DIM)
            mx = jnp.maximum(jnp.maximum(s1[j], s2[j]), s3[j])
            w1 = jnp.exp(s1[j] - mx)
            w2 = jnp.exp(s2[j] - mx)
            w3 = jnp.exp(s3[j] - mx)
            den = w1 + w2 + w3
            a = (w1 * o1[j] + w2 * o2[j] + w3 * o3[j]) / den
            a_ref[:, sl] = a
            ab_ref[:, sl] = a.astype(BF16)
            lse_ref[:, sl] = mx + jnp.log(den)

    heads = pl.BlockSpec((HEADS_PER_GROUP, ROW_TILE, HEAD_DIM), lambda i: (0, i, 0))
    return pl.pallas_call(
        body, name="attn_merge", grid=(S // ROW_TILE,),
        in_specs=[heads] * 6, out_specs=[_row_spec(GROUP_WIDTH)] * 3,
        out_shape=[jax.ShapeDtypeStruct((S, GROUP_WIDTH), F32), jax.ShapeDtypeStruct((S, GROUP_WIDTH), BF16),
                   jax.ShapeDtypeStruct((S, GROUP_WIDTH), F32)],
        compiler_params=_params(("parallel",)),
    )(*parts)


def _attn_delta_call(a, da):
    S = a.shape[0]

    def body(a_ref, da_ref, d_ref):
        for j in range(HEADS_PER_GROUP):
            sl = slice(j * HEAD_DIM, (j + 1) * HEAD_DIM)
            d = jnp.sum(a_ref[:, sl] * da_ref[:, sl], axis=-1, keepdims=True)
            d_ref[:, sl] = jnp.broadcast_to(d, (ROW_TILE, HEAD_DIM))

    return pl.pallas_call(
        body, name="attn_delta", grid=(S // ROW_TILE,),
        in_specs=[_row_spec(GROUP_WIDTH)] * 2, out_specs=_row_spec(GROUP_WIDTH),
        out_shape=jax.ShapeDtypeStruct((S, GROUP_WIDTH), F32),
        compiler_params=_params(("parallel",)),
    )(a, da)


def _attn_bwd_call(proj, bias, da, lse, delta, group):
    S = proj.shape[0]
    r, nb, hp, per = _attn_plan(S, group)
    scale = HEAD_DIM ** -0.5
    kinds = ("q", "qn", "kp", "kc", "vp", "vc", "da", "dan", "lse", "lsen", "dl", "dln") if nb > 1 else ("q", "kc", "vc", "da", "lse", "dl")
    source = dict(da=da, dan=da, lse=lse, lsen=lse, dl=delta, dln=delta)

    per_kind = _refs_per_kind(r, hp)

    def body(*refs):
        ins = {kind: refs[i * per_kind:(i + 1) * per_kind] for i, kind in enumerate(kinds)}
        b_ref, dq_ref, dk_ref, dv_ref, db_ref = refs[len(kinds) * per_kind:]
        n = pl.program_id(1)
        prev_ok, cur_ok, next_ok = _band_masks(n, nb)

        @pl.when(n == 0)
        def _():
            db_ref[...] = jnp.zeros_like(db_ref)

        band_ok = _wide_band_mask(n) if nb > 1 else cur_ok

        def residue(rho):
            rows = _residue_rows(rho, r)
            for j in range(hp):
                get = lambda kind: _head_rows(ins[kind], j, rows, r)
                q = get("q").astype(BF16)
                kc = get("kc").astype(BF16)
                vc = get("vc").astype(BF16)
                dav = get("da").astype(BF16)
                lse_q, dl_q = get("lse"), get("dl")
                if nb == 1:
                    pc = jnp.exp(jnp.where(cur_ok, _dot_nt(q, kc) * scale + b_ref[j, :, SPAN:], NEG_INF) - lse_q)
                    dsc = pc * (_dot_nt(dav, vc) - dl_q)
                    dsc_b = dsc.astype(BF16)
                    dq = _dot_nn(dsc_b, kc)
                    dk = _dot_tn(dsc_b, q)
                    dv = _dot_tn(pc.astype(BF16), dav)
                    db_ref[j, :, SPAN:] += dsc
                else:
                    qn = get("qn").astype(BF16)
                    dan = get("dan").astype(BF16)
                    keys = jnp.concatenate([get("kp").astype(BF16), kc], axis=0)
                    vals = jnp.concatenate([get("vp").astype(BF16), vc], axis=0)
                    wide = lambda t: jnp.concatenate([t, t], axis=1)
                    p = jnp.exp(jnp.where(band_ok, _dot_nt(q, keys) * scale + b_ref[j], NEG_INF) - wide(lse_q))
                    ds = p * (_dot_nt(dav, vals) - wide(dl_q))
                    dq = _dot_nn(ds.astype(BF16), keys)
                    db_ref[j] += ds
                    pn = jnp.exp(jnp.where(next_ok, _dot_nt(qn, kc) * scale + b_ref[j, :, :SPAN], NEG_INF) - get("lsen"))
                    dsn = pn * (_dot_nt(dan, vc) - get("dln"))
                    both = lambda cur_part, next_part: jnp.concatenate([cur_part.astype(BF16), next_part.astype(BF16)], axis=0)
                    dk = _dot_tn(both(ds[:, SPAN:], dsn), jnp.concatenate([q, qn], axis=0))
                    dv = _dot_tn(both(p[:, SPAN:], pn), jnp.concatenate([dav, dan], axis=0))
                dq_ref[j, rows, :] = dq * scale
                dk_ref[j, rows, :] = dk * scale
                dv_ref[j, rows, :] = dv

        _for_residues(r, per, residue)

    per_group = HEADS_PER_GROUP // hp
    band = (hp, SPAN, 2 * SPAN)
    in_specs = [_head_spec(r, nb, hp, kind, group, jj) for kind in kinds for jj in range(per_kind)]
    in_specs.append(pl.BlockSpec(band, lambda j, n: (group * per_group + j, 0, 0)))
    operands = [source.get(kind, proj) for kind in kinds for _ in range(per_kind)] + [bias]
    out = pl.BlockSpec((hp, r * SPAN, HEAD_DIM), lambda j, n: (j, n, 0))
    return pl.pallas_call(
        body, name=f"attn_bwd_g{group}", grid=(per_group, nb),
        in_specs=in_specs,
        out_specs=[out] * 3 + [pl.BlockSpec(band, lambda j, n: (j, 0, 0))],
        out_shape=[jax.ShapeDtypeStruct((HEADS_PER_GROUP, S, HEAD_DIM), F32)] * 3
        + [jax.ShapeDtypeStruct((HEADS_PER_GROUP, SPAN, 2 * SPAN), F32)],
        compiler_params=_params(("parallel", "arbitrary"), VMEM_LIMIT),
    )(*operands)


def _dproj_call(dqkv, tails):
    S = tails[0].shape[0]
    width = len(dqkv) * GROUP_WIDTH + sum(t.shape[1] for t in tails)

    def body(*refs):
        o_ref = refs[-1]
        col = 0
        for ref in refs[:len(dqkv)]:
            for j in range(HEADS_PER_GROUP):
                o_ref[:, col:col + HEAD_DIM] = ref[j].astype(BF16)
                col += HEAD_DIM
        for ref in refs[len(dqkv):-1]:
            o_ref[:, col:col + ref.shape[1]] = ref[...]
            col += ref.shape[1]

    heads = pl.BlockSpec((HEADS_PER_GROUP, ROW_TILE, HEAD_DIM), lambda i: (0, i, 0))
    return pl.pallas_call(
        body, name="dproj_assemble", grid=(S // ROW_TILE,),
        in_specs=[heads] * len(dqkv) + [_row_spec(t.shape[1]) for t in tails],
        out_specs=_row_spec(width), out_shape=jax.ShapeDtypeStruct((S, width), BF16),
        compiler_params=_params(("parallel",)),
    )(*dqkv, *tails)


def _tap_rows(xpad_ref, t0, k, width, pad):
    return xpad_ref[pl.ds(t0 + (pad - (width - 1 - k)), TIME_BLOCK), :]


def _conv_block(xpad_ref, t0, w_ref, width, pad):
    acc = None
    for k in range(width):
        term = w_ref[k:k + 1, :] * _tap_rows(xpad_ref, t0, k, width, pad)
        acc = term if acc is None else acc + term
    return acc


def _conv_transpose_block(dpad_ref, t0, w_ref, width):
    acc = None
    for k in range(width):
        term = w_ref[k:k + 1, :] * dpad_ref[pl.ds(t0 + (width - 1 - k), TIME_BLOCK), :]
        acc = term if acc is None else acc + term
    return acc


def _conv_weight_grad(xpad_ref, t0, dy, dw_ref, width, pad):
    for k in range(width):
        dw_ref[k:k + 1, :] += jnp.sum(dy * _tap_rows(xpad_ref, t0, k, width, pad), axis=0, keepdims=True)


def _time_loop(S, step, skip_first=0, skip_last=0):
    def it(tb, carry):
        step(pl.multiple_of(tb * TIME_BLOCK, TIME_BLOCK))
        return carry

    lax.fori_loop(skip_first, S // TIME_BLOCK - skip_last, it, 0)


def _fill_head(head_ref, x_ref, pad):
    head_ref[0:pad, :] = jnp.zeros((pad, LANES), F32)
    head_ref[pad:, :] = x_ref[0:TIME_BLOCK, :]


def _fill_tail(tail_ref, x_ref, pad):
    S = x_ref.shape[0]
    tail_ref[0:TIME_BLOCK, :] = x_ref[S - TIME_BLOCK:S, :]
    tail_ref[TIME_BLOCK:, :] = jnp.zeros((pad, LANES), F32)


def _conv_fwd_call(proj, col0, w, b):
    S = proj.shape[0]
    C = w.shape[1]
    nt = C // LANES
    v0, g0 = col0 // LANES, (col0 + C) // LANES

    def body(val_ref, gate_ref, w_ref, b_ref, o_ref, pad_ref):
        pad_ref[0:CONV_PAD, :] = jnp.zeros((CONV_PAD, LANES), F32)
        pad_ref[CONV_PAD:, :] = val_ref[...] * _sigmoid(gate_ref[...])

        def step(t0):
            o_ref[pl.ds(t0, TIME_BLOCK), :] = _conv_block(pad_ref, t0, w_ref, CONV_WIDTH, CONV_PAD) + b_ref[...]

        _time_loop(S, step)

    seq = lambda off: pl.BlockSpec((S, LANES), lambda i: (0, off + i))
    return pl.pallas_call(
        body, name="conv_module", grid=(nt,),
        in_specs=[seq(v0), seq(g0), pl.BlockSpec((CONV_WIDTH, LANES), lambda i: (0, i)), pl.BlockSpec((1, LANES), lambda i: (0, i))],
        out_specs=seq(0), out_shape=jax.ShapeDtypeStruct((S, C), F32),
        scratch_shapes=[pltpu.VMEM((S + CONV_PAD, LANES), F32)],
        compiler_params=_params(("parallel",)),
    )(proj, proj, w, b)


def _conv_bwd_call(proj, col0, w, dc1):
    S = proj.shape[0]
    C = w.shape[1]
    nt = C // LANES
    v0, g0 = col0 // LANES, (col0 + C) // LANES

    def body(val_ref, gate_ref, w_ref, dy_ref, dval_ref, dgate_ref, dw_ref, db_ref, xpad_ref, tail_ref, dwacc_ref):
        xpad_ref[0:CONV_PAD, :] = jnp.zeros((CONV_PAD, LANES), F32)
        xpad_ref[CONV_PAD:, :] = val_ref[...] * _sigmoid(gate_ref[...])
        _fill_tail(tail_ref, dy_ref, CONV_PAD)
        dwacc_ref[...] = jnp.zeros_like(dwacc_ref)

        def block(t0, dy_src, dy_t0):
            rows = pl.ds(t0, TIME_BLOCK)
            _conv_weight_grad(xpad_ref, t0, dy_ref[rows, :], dwacc_ref, CONV_WIDTH, CONV_PAD)
            dc0 = _conv_transpose_block(dy_src, dy_t0, w_ref, CONV_WIDTH)
            sg = _sigmoid(gate_ref[rows, :])
            dval_ref[rows, :] = (dc0 * sg).astype(BF16)
            dgate_ref[rows, :] = (dc0 * val_ref[rows, :] * sg * (1.0 - sg)).astype(BF16)

        _time_loop(S, lambda t0: block(t0, dy_ref, t0), skip_last=1)
        block(S - TIME_BLOCK, tail_ref, 0)
        dw_ref[...] = dwacc_ref[...]
        db_ref[...] = jnp.sum(dy_ref[...], axis=0, keepdims=True)

    seq = lambda off: pl.BlockSpec((S, LANES), lambda i: (0, off + i))
    return pl.pallas_call(
        body, name="conv_module_bwd", grid=(nt,),
        in_specs=[seq(v0), seq(g0), pl.BlockSpec((CONV_WIDTH, LANES), lambda i: (0, i)), seq(0)],
        out_specs=[seq(0), seq(0), pl.BlockSpec((CONV_PAD, LANES), lambda i: (0, i)), pl.BlockSpec((1, LANES), lambda i: (0, i))],
        out_shape=[jax.ShapeDtypeStruct((S, C), BF16), jax.ShapeDtypeStruct((S, C), BF16),
                   jax.ShapeDtypeStruct((CONV_PAD, C), F32), jax.ShapeDtypeStruct((1, C), F32)],
        scratch_shapes=[pltpu.VMEM((S + CONV_PAD, LANES), F32), pltpu.VMEM((TIME_BLOCK + CONV_PAD, LANES), F32),
                        pltpu.VMEM((CONV_PAD, LANES), F32)],
        compiler_params=_params(("parallel",)),
    )(proj, proj, w, dc1)


def _ffn_fwd_call(u, w, b):
    S, C2 = u.shape
    C = C2 // 2
    nt = C // LANES

    def body(ug_ref, uv_ref, wg_ref, wv_ref, bg_ref, bv_ref, f_ref, hg_ref, hv_ref):
        _fill_head(hg_ref, ug_ref, FFN_PAD)
        _fill_head(hv_ref, uv_ref, FFN_PAD)

        def block(t0, xg_ref, xv_ref, x_t0, pad):
            cg = _conv_block(xg_ref, x_t0, wg_ref, FFN_CONV_WIDTH, pad) + bg_ref[...]
            cv = _conv_block(xv_ref, x_t0, wv_ref, FFN_CONV_WIDTH, pad) + bv_ref[...]
            f_ref[pl.ds(t0, TIME_BLOCK), :] = (_gelu(cg) * cv).astype(BF16)

        block(0, hg_ref, hv_ref, 0, FFN_PAD)
        _time_loop(S, lambda t0: block(t0, ug_ref, uv_ref, t0, 0), skip_first=1)

    seq = lambda off: pl.BlockSpec((S, LANES), lambda i: (0, off + i))
    wsp = lambda off: pl.BlockSpec((FFN_CONV_WIDTH, LANES), lambda i: (0, off + i))
    bsp = lambda off: pl.BlockSpec((1, LANES), lambda i: (0, off + i))
    return pl.pallas_call(
        body, name="ffn_conv_geglu", grid=(nt,),
        in_specs=[seq(0), seq(nt), wsp(0), wsp(nt), bsp(0), bsp(nt)],
        out_specs=seq(0), out_shape=jax.ShapeDtypeStruct((S, C), BF16),
        scratch_shapes=[pltpu.VMEM((FFN_PAD + TIME_BLOCK, LANES), F32)] * 2,
        compiler_params=_params(("parallel",)),
    )(u, u, w, w, b, b)


def _ffn_bwd_call(u, w, b, df):
    S, C2 = u.shape
    C = C2 // 2
    nt = C // LANES

    def body(ug_ref, uv_ref, wg_ref, wv_ref, bg_ref, bv_ref, df_ref,
             du_ref, dwg_ref, dwv_ref, dbg_ref, dbv_ref,
             hg_ref, hv_ref, dg_ref, dv_ref, dwg_acc, dwv_acc, dbg_acc, dbv_acc):
        zeros = jnp.zeros((FFN_PAD, LANES), F32)
        _fill_head(hg_ref, ug_ref, FFN_PAD)
        _fill_head(hv_ref, uv_ref, FFN_PAD)
        dg_ref[S:, :] = zeros
        dv_ref[S:, :] = zeros
        dwg_acc[...] = jnp.zeros_like(dwg_acc)
        dwv_acc[...] = jnp.zeros_like(dwv_acc)
        dbg_acc[...] = jnp.zeros_like(dbg_acc)
        dbv_acc[...] = jnp.zeros_like(dbv_acc)

        def first(t0, xg_ref, xv_ref, x_t0, pad):
            rows = pl.ds(t0, TIME_BLOCK)
            cg = _conv_block(xg_ref, x_t0, wg_ref, FFN_CONV_WIDTH, pad) + bg_ref[...]
            cv = _conv_block(xv_ref, x_t0, wv_ref, FFN_CONV_WIDTH, pad) + bv_ref[...]
            dfb = df_ref[rows, :]
            gelu, gelu_grad = _gelu_and_grad(cg)
            dcg = dfb * cv * gelu_grad
            dcv = dfb * gelu
            dg_ref[rows, :] = dcg
            dv_ref[rows, :] = dcv
            _conv_weight_grad(xg_ref, x_t0, dcg, dwg_acc, FFN_CONV_WIDTH, pad)
            _conv_weight_grad(xv_ref, x_t0, dcv, dwv_acc, FFN_CONV_WIDTH, pad)
            dbg_acc[...] += jnp.sum(dcg, axis=0, keepdims=True)
            dbv_acc[...] += jnp.sum(dcv, axis=0, keepdims=True)

        def second(t0):
            rows = pl.ds(t0, TIME_BLOCK)
            du_ref[0, rows, :] = _conv_transpose_block(dg_ref, t0, wg_ref, FFN_CONV_WIDTH).astype(BF16)
            du_ref[1, rows, :] = _conv_transpose_block(dv_ref, t0, wv_ref, FFN_CONV_WIDTH).astype(BF16)

        first(0, hg_ref, hv_ref, 0, FFN_PAD)
        _time_loop(S, lambda t0: first(t0, ug_ref, uv_ref, t0, 0), skip_first=1)
        _time_loop(S, second)
        dwg_ref[...] = dwg_acc[...]
        dwv_ref[...] = dwv_acc[...]
        dbg_ref[...] = dbg_acc[...]
        dbv_ref[...] = dbv_acc[...]

    seq = lambda off: pl.BlockSpec((S, LANES), lambda i: (0, off + i))
    wsp = lambda off: pl.BlockSpec((FFN_CONV_WIDTH, LANES), lambda i: (0, off + i))
    bsp = lambda off: pl.BlockSpec((1, LANES), lambda i: (0, off + i))
    return pl.pallas_call(
        body, name="ffn_conv_geglu_bwd", grid=(nt,),
        in_specs=[seq(0), seq(nt), wsp(0), wsp(nt), bsp(0), bsp(nt), seq(0)],
        out_specs=[pl.BlockSpec((2, S, LANES), lambda i: (0, 0, i)),
                   pl.BlockSpec((SUBLANES, LANES), lambda i: (0, i)), pl.BlockSpec((SUBLANES, LANES), lambda i: (0, i)),
                   bsp(0), bsp(0)],
        out_shape=[jax.ShapeDtypeStruct((2, S, C), BF16)] + [jax.ShapeDtypeStruct((SUBLANES, C), F32)] * 2
        + [jax.ShapeDtypeStruct((1, C), F32)] * 2,
        scratch_shapes=[pltpu.VMEM((FFN_PAD + TIME_BLOCK, LANES), F32)] * 2 + [pltpu.VMEM((S + FFN_PAD, LANES), F32)] * 2
        + [pltpu.VMEM((SUBLANES, LANES), F32)] * 2
        + [pltpu.VMEM((1, LANES), F32)] * 2,
        compiler_params=_params(("parallel",)),
    )(u, u, w, w, b, b, df)


def _adamw(w_ref, g_ref, m_ref, v_ref, d_ref, mo_ref, vo_ref):
    gv = g_ref[...]
    mn = ADAM_B1 * m_ref[...] + (1.0 - ADAM_B1) * gv
    vn = ADAM_B2 * v_ref[...] + (1.0 - ADAM_B2) * (gv * gv)
    mo_ref[...] = mn
    vo_ref[...] = vn
    m_hat = mn * (1.0 / (1.0 - ADAM_B1 ** ADAM_STEP))
    v_hat = vn * (1.0 / (1.0 - ADAM_B2 ** ADAM_STEP))
    d_ref[...] = -ADAM_LR * (m_hat / (jnp.sqrt(v_hat) + ADAM_EPS) + ADAM_WD * w_ref[...])


def _adamw_call(w, g, m, v, name):
    R, C = w.shape
    tr = _row_tile(R, C)

    def body(w_ref, g_ref, m_ref, v_ref, go_ref, d_ref, mo_ref, vo_ref):
        go_ref[...] = g_ref[...]
        _adamw(w_ref, g_ref, m_ref, v_ref, d_ref, mo_ref, vo_ref)

    spec = pl.BlockSpec((tr, C), lambda i: (i, 0))
    return pl.pallas_call(
        body, name=name, grid=(R // tr,),
        in_specs=[spec] * 4, out_specs=[spec] * 4,
        out_shape=[jax.ShapeDtypeStruct((R, C), F32)] * 4,
        compiler_params=_params(("parallel",)),
    )(w, g, m, v)


def _adamw_small_call(ws, gs, ms, vs):
    n = len(ws)

    def body(*refs):
        w_refs, g_refs, m_refs, v_refs, d_refs, mo_refs, vo_refs = (refs[i * n:(i + 1) * n] for i in range(7))
        for i in range(n):
            _adamw(w_refs[i], g_refs[i], m_refs[i], v_refs[i], d_refs[i], mo_refs[i], vo_refs[i])

    whole = pl.BlockSpec(memory_space=pltpu.VMEM)
    outs = pl.pallas_call(
        body, name="adamw_small",
        in_specs=[whole] * (4 * n), out_specs=[whole] * (3 * n),
        out_shape=[jax.ShapeDtypeStruct(w.shape, F32) for w in ws] * 3,
    )(*ws, *gs, *ms, *vs)
    return outs[:n], outs[n:2 * n], outs[2 * n:]


def _position():
    return lax.axis_index("x"), lax.axis_index("y"), lax.axis_index("c")


def _chip_peers(x, y):
    return [(x, 1 - y), (1 - x, y), (1 - x, 1 - y)]


def _half_rows(ref, core, rows):
    h = rows // 2
    start = pl.multiple_of(core * h, PACKED_ROWS)
    return ref.at[pl.ds(start, h), :] if len(ref.shape) == 2 else ref.at[:, pl.ds(start, h), :]


def _shard_half(ref, shard, core, rows):
    h = rows // 2
    return ref.at[shard, pl.ds(pl.multiple_of(core * h, PACKED_ROWS), h), :]


ANY = pl.BlockSpec(memory_space=pl.ANY)


def _first_hop_copies(srcs, lands):
    x, y, c = _position()
    chip = 2 * x + y
    targets = [(px, py, c) for px, py in _chip_peers(x, y)] + [(x, y, 1 - c)]
    rows = srcs[0].shape[0]
    out = []
    for i, (s, l) in enumerate(zip(srcs, lands)):
        for k, dev in enumerate(targets):
            if i == 0 and k < 3:
                out.append((_half_rows(s, c, rows), _shard_half(l, chip, c, rows), dev, k))
            else:
                out.append((s, l.at[chip], dev, len(targets) * i + k))
    return out


def _second_hop_copies(srcs, lands):
    x, y, c = _position()
    rows = lands[0].shape[1]
    out = []
    for k, (px, py) in enumerate(_chip_peers(x, y)):
        half = _shard_half(lands[0], 2 * px + py, c, rows)
        out.append((half, half, (x, y, 1 - c), k))
    return out


HBM_SPEC = pl.BlockSpec(memory_space=pltpu.HBM)
SEM_SPEC = pl.BlockSpec(memory_space=pltpu.SEMAPHORE)
DATAFLOW = pltpu.SideEffectType.DATAFLOW_SIDE_EFFECTING


def _in_hbm(a):
    return pltpu.with_memory_space_constraint(a, pltpu.HBM)


def _split_start(name, groups, after, carry=None):
    spans, arrays = [], []
    for srcs, lands, _, _ in groups:
        spans.append((len(arrays), len(srcs), len(lands)))
        arrays += list(srcs) + list(lands)
    if carry is not None:
        arrays.append(carry)
    na, ng = len(arrays), len(groups)

    def body(*refs):
        sems, token = refs[na + 1:na + 1 + 2 * ng], refs[-1]
        for g, (_, _, _, copies) in enumerate(groups):
            off, ns, nl = spans[g]
            for src, dst, dev, idx in copies(refs[off:off + ns], refs[off + ns:off + ns + nl]):
                pltpu.make_async_remote_copy(src_ref=src, dst_ref=dst, send_sem=sems[2 * g].at[idx], recv_sem=sems[2 * g + 1].at[idx],
                                             device_id=dev, device_id_type=MESH).start()
        token[...] = jnp.zeros_like(token)

    outs = pl.pallas_call(
        body, name=name,
        in_specs=[HBM_SPEC] * na + [ANY],
        out_specs=[SEM_SPEC] * (2 * ng) + [HBM_SPEC] * na + [pl.BlockSpec(memory_space=pltpu.VMEM)],
        out_shape=[pltpu.SemaphoreType.DMA((n_sems,)) for _, _, n_sems, _ in groups for _ in range(2)]
        + [pltpu.HBM(a.shape, a.dtype) for a in arrays] + [jax.ShapeDtypeStruct((SUBLANES, LANES), F32)],
        input_output_aliases={i: 2 * ng + i for i in range(na)},
        compiler_params=pltpu.CompilerParams(has_side_effects=DATAFLOW),
    )(*[_in_hbm(a) for a in arrays], after)
    started = []
    for g, (off, ns, nl) in enumerate(spans):
        thru = outs[2 * ng + off:2 * ng + off + ns + nl]
        started.append(dict(send=outs[2 * g], recv=outs[2 * g + 1], srcs=list(thru[:ns]), lands=list(thru[ns:]),
                            tile=outs[-1], token=outs[-1][0, 0], carry=None if carry is None else outs[2 * ng + na - 1]))
    return started


def _split_wait(name, started, copies, after):
    n, m = len(started["srcs"]), len(started["lands"])

    def body(*refs):
        src_refs, land_refs = refs[:n], refs[n:n + m]
        send_sem, recv_sem = refs[n + m], refs[n + m + 1]
        for src, dst, dev, idx in copies(src_refs, land_refs):
            cp = pltpu.make_async_remote_copy(src_ref=src, dst_ref=dst, send_sem=send_sem.at[idx], recv_sem=recv_sem.at[idx],
                                              device_id=dev, device_id_type=MESH)
            cp.wait_send()
            cp.wait_recv()

    arrays = started["srcs"] + started["lands"]
    outs = pl.pallas_call(
        body, name=name,
        in_specs=[HBM_SPEC] * (n + m) + [SEM_SPEC, SEM_SPEC, ANY],
        out_specs=[HBM_SPEC] * (n + m),
        out_shape=[pltpu.HBM(a.shape, a.dtype) for a in arrays],
        input_output_aliases={i: i for i in range(n + m)},
        compiler_params=pltpu.CompilerParams(has_side_effects=DATAFLOW),
    )(*arrays, started["send"], started["recv"], after)
    return list(outs)


def _gather_copies(srcs, lands):
    x, y, c = _position()
    chip = 2 * x + y
    targets = [(px, py, c) for px, py in _chip_peers(x, y)] + [(x, y, 1 - c)]
    return [(s, l.at[chip], dev, len(targets) * i + k) for i, (s, l) in enumerate(zip(srcs, lands)) for k, dev in enumerate(targets)]


def _sibling_copies(srcs, lands):
    x, y, c = _position()
    return [(_half_rows(srcs[0], 1 - c, srcs[0].shape[1]), lands[0], (x, y, 1 - c), 0)]


def _sibling_whole_copies(srcs, lands):
    x, y, c = _position()
    return [(srcs[0], lands[0], (x, y, 1 - c), 0)]


def _exchange_copies(srcs, lands):
    x, y, c = _position()
    return [(srcs[0].at[2 * px + py], lands[0].at[k], (px, py, c), k) for k, (px, py) in enumerate(_chip_peers(x, y))]


def _pair_sum_call(grad, recv, core, name):
    _, h, B = recv.shape
    tr = _row_tile(h, B)

    def body(core_ref, g_ref, r_ref, o_ref, ob_ref):
        s = g_ref[...] + r_ref[...]
        o_ref[...] = s
        ob_ref[...] = s.astype(BF16)

    g_spec = pl.BlockSpec((None, tr, B), lambda q, i, core_ref: (q, core_ref[0] * (h // tr) + i, 0))
    spec = pl.BlockSpec((None, tr, B), lambda q, i, core_ref: (q, i, 0))
    return pl.pallas_call(
        body, name=name,
        grid_spec=pltpu.PrefetchScalarGridSpec(num_scalar_prefetch=1, grid=(N_CHIPS, h // tr), in_specs=[g_spec, spec],
                                               out_specs=[spec, spec]),
        out_shape=[jax.ShapeDtypeStruct(recv.shape, F32), jax.ShapeDtypeStruct(recv.shape, BF16)],
        compiler_params=_params(("parallel", "parallel")),
    )(core, grad, recv)


def _chip_sum_call(partial, recv, chip_core, name):
    _, h, B = recv.shape
    tr = _row_tile(h, B)

    def body(cc_ref, p_ref, r_ref, o_ref):
        o_ref[...] = ((p_ref[...] + r_ref[0].astype(F32)) + r_ref[1].astype(F32)) + r_ref[2].astype(F32)

    return pl.pallas_call(
        body, name=name,
        grid_spec=pltpu.PrefetchScalarGridSpec(
            num_scalar_prefetch=1, grid=(h // tr,),
            in_specs=[pl.BlockSpec((None, tr, B), lambda i, cc_ref: (cc_ref[0], i, 0)),
                      pl.BlockSpec((3, tr, B), lambda i, cc_ref: (0, i, 0))],
            out_specs=pl.BlockSpec((tr, B), lambda i, cc_ref: (cc_ref[1] * (h // tr) + i, 0))),
        out_shape=jax.ShapeDtypeStruct((2 * h, B), F32),
        compiler_params=_params(("parallel",)),
    )(chip_core, partial, recv)


def _sibling_assemble_call(shards, name="grad_sibling_assemble"):
    n = len(shards)

    def body(*refs):
        ins, outs = refs[:n], refs[n:2 * n]
        send_sems, recv_sems = refs[2 * n:]
        x, y, c = _position()
        copies = []
        for i in range(n):
            rows = shards[i].shape[0]
            cp = pltpu.make_async_remote_copy(src_ref=_half_rows(ins[i], c, rows), dst_ref=_half_rows(outs[i], c, rows),
                                              send_sem=send_sems.at[i], recv_sem=recv_sems.at[i],
                                              device_id=(x, y, 1 - c), device_id_type=MESH)
            cp.start()
            copies.append(cp)
        for cp in copies:
            cp.wait()

    return pl.pallas_call(
        body, name=name,
        in_specs=[ANY] * n, out_specs=[ANY] * n,
        out_shape=[jax.ShapeDtypeStruct(s.shape, F32) for s in shards],
        input_output_aliases={i: i for i in range(n)},
        scratch_shapes=[pltpu.SemaphoreType.DMA((n,)), pltpu.SemaphoreType.DMA((n,))],
    )(*shards)


N_DEVICES = 8


def _allsum_copies(srcs, lands):
    x, y, c = _position()
    me = 4 * x + 2 * y + c
    out = []
    for k in range(1, N_DEVICES):
        peer = (1 - x if k & 4 else x, 1 - y if k & 2 else y, 1 - c if k & 1 else c)
        out.append((srcs[0], lands[0].at[me], peer, k - 1))
    return out


def _ordered_sum_call(mine, landed, me_chip, shapes, sharded_cols):
    rows = mine.shape[0]
    outs = [(s[0], n) if n else s for s, n in zip(shapes, sharded_cols)]

    def body(mc_ref, x_ref, l_ref, *refs):
        acc_ref = refs[-1]
        acc = jnp.where(mc_ref[0] == 0, x_ref[...], l_ref[0])
        for d in range(1, N_DEVICES):
            acc = acc + jnp.where(mc_ref[0] == d, x_ref[...], l_ref[d])
        acc_ref[...] = acc
        first = 0
        for o_ref, (r, c), n in zip(refs[:-1], shapes, sharded_cols):
            per_row = c // LANES

            def unpack(chip, o_ref=o_ref, r=r, n=n, per_row=per_row, first=first):
                for i in range(r):
                    for j in range((n or per_row * LANES) // LANES):
                        src = first + i * per_row + chip * ((n or 0) // LANES) + j
                        o_ref[i:i + 1, j * LANES:(j + 1) * LANES] = acc_ref[src:src + 1, :]

            if n:
                for q in range(N_CHIPS):
                    pl.when(mc_ref[1] == q)(functools.partial(unpack, q))
            else:
                unpack(0)
            first += r * per_row

    results = pl.pallas_call(
        body, name="small_grad_sum",
        in_specs=[pl.BlockSpec(memory_space=pltpu.SMEM), pl.BlockSpec(memory_space=pltpu.VMEM), pl.BlockSpec(memory_space=pltpu.VMEM)],
        out_specs=[pl.BlockSpec(memory_space=pltpu.VMEM)] * len(outs),
        out_shape=[jax.ShapeDtypeStruct(s, F32) for s in outs],
        scratch_shapes=[pltpu.VMEM((rows, LANES), F32)],
    )(me_chip, mine, landed)
    return results


def _pack(arrays):
    flat = jnp.concatenate([a.reshape(-1).astype(F32) for a in arrays])
    rows = -(-flat.shape[0] // LANES)
    rows = -(-rows // SUBLANES) * SUBLANES
    flat = jnp.pad(flat, (0, rows * LANES - flat.shape[0]))
    return flat.reshape(rows, LANES)


def _local_step(xs, target, P, late_weights, on_grad):
    S, D = xs.shape
    qkv_width = 3 * N_HEADS * HEAD_DIM
    glu_col0, gate_col0 = qkv_width, qkv_width + 2 * D
    shard_major = lambda g: g.reshape(N_CHIPS, g.shape[0] // N_CHIPS, g.shape[1])

    h1 = _rms_fwd_call(xs, P["norm_mix_pre"])
    buckets = _bucket_tables()
    bias = _bias_table_call(P["rel_bias"] + 0.0 * h1[0, 0].astype(F32), buckets)
    P = dict(P, **late_weights("in", bias))
    proj = _matmul(h1, P["w_in"], "nn", "proj_in")
    parts = []
    for g in range(N_GROUPS):
        parts += _attn_fwd_call(proj, bias, g)
    a, a_bf, lse = _attn_merge_call(parts)
    P = dict(P, **late_weights("mix", a_bf))
    y_a = _matmul(a_bf, P["w_attn_out"], "nn", "attn_out")
    c1 = _conv_fwd_call(proj, glu_col0, P["conv_dw_w"], P["conv_dw_b"])
    cact = _ln_silu_call(c1, P["conv_ln_g"], P["conv_ln_b"])
    y_c = _matmul(cact, P["conv_pw_w"], "nn", "conv_pw")
    mixed = _mix_call(proj, gate_col0, P["b_gate"], y_a, y_c)
    out = _matmul(mixed, P["w_out"], "nn", "mix_out")
    x1, h2 = _res1_call(xs, out, P["norm_mix_post"], P["norm_ffn_pre"])
    P = dict(P, **late_weights("up", h2))
    u = _matmul(h2, P["w_up"], "nn", "ffn_up")
    f = _ffn_fwd_call(u, P["ffn_conv_w"], P["ffn_conv_b"])
    P = dict(P, **late_weights("down", f))
    yff = _matmul(f, P["w_down"], "nn", "ffn_down")
    loss_tile, dx2, dyff, dg_ffn_post = _loss_call(yff, x1, P["norm_ffn_post"], target)

    G = {}
    G["norm_ffn_post"] = dg_ffn_post
    on_grad("w_down", shard_major(_matmul(f, dyff, "tn", "ffn_down_dw")))
    df = _matmul(dyff, P["w_down"], "nt", "ffn_down_dx")
    du, dwg, dwv, dbg, dbv = _ffn_bwd_call(u, P["ffn_conv_w"], P["ffn_conv_b"], df)
    G["ffn_conv_w"] = jnp.concatenate([dwg[:FFN_CONV_WIDTH], dwv[:FFN_CONV_WIDTH]], axis=1)
    G["ffn_conv_b"] = jnp.concatenate([dbg, dbv], axis=1)
    du = on_grad("w_up", functools.partial(_grad_half_matmul, h2, "ffn_up_dw"), carry=du)
    dh2 = _matmul(du, P["w_up"], "nt", "ffn_up_dx")
    dx1, dout, G["norm_ffn_pre"], G["norm_mix_post"] = _mid_bwd_call(x1, P["norm_ffn_pre"], dh2, dx2, out, P["norm_mix_post"])
    on_grad("w_out", shard_major(_matmul(mixed, dout, "tn", "mix_out_dw")))
    dmixed = _matmul(dout, P["w_out"], "nt", "mix_out_dx")
    dya, dyc, dga, dgc, dba, dbc = _mix_bwd_call(dmixed, proj, gate_col0, P["b_gate"], y_a, y_c)
    G["b_gate"] = jnp.concatenate([dba, dbc], axis=1)
    on_grad("w_attn_out", _matmul(a_bf, dya, "tn", "attn_out_dw", out_shards=True))
    dyc = on_grad("conv_pw_w", shard_major(_matmul(cact, dyc, "tn", "conv_pw_dw")), carry=dyc)
    da = _matmul(dya, P["w_attn_out"], "nt", "attn_out_dx")
    dcact = _matmul(dyc, P["conv_pw_w"], "nt", "conv_pw_dx")
    dc1, G["conv_ln_g"], G["conv_ln_b"] = _ln_silu_bwd_call(c1, P["conv_ln_g"], P["conv_ln_b"], dcact)
    dval, dgate, dw_dw, G["conv_dw_b"] = _conv_bwd_call(proj, glu_col0, P["conv_dw_w"], dc1)
    G["conv_dw_w"] = dw_dw[:CONV_WIDTH]
    delta = _attn_delta_call(a, da)
    dqs, dks, dvs, dbs = [], [], [], []
    for g in range(N_GROUPS):
        dq, dk, dv, db = _attn_bwd_call(proj, bias, da, lse, delta, g)
        dqs.append(dq)
        dks.append(dk)
        dvs.append(dv)
        dbs.append(db)
    G["rel_bias"] = _bias_grad_call(jnp.concatenate(dbs, axis=0), buckets)
    dproj = _dproj_call(dqs + dks + dvs, [dval, dgate, dga, dgc])
    dproj = on_grad("w_in", functools.partial(_grad_half_matmul, h1, "proj_in_dw"), carry=dproj)
    dh1 = _matmul(dproj, P["w_in"], "nt", "proj_in_dx")
    dh1 = on_grad(None, None, carry=dh1)
    grad_x, G["norm_mix_pre"] = _in_bwd_call(xs, P["norm_mix_pre"], dh1, dx1)
    return loss_tile, grad_x, G


def kernel(x, w_in, b_gate, rel_bias, w_attn_out, conv_dw_w, conv_dw_b, conv_ln_g, conv_ln_b, conv_pw_w, w_out, norm_mix_pre, norm_mix_post, norm_ffn_pre, norm_ffn_post, w_up, ffn_conv_w, ffn_conv_b, w_down, loss_target, m_w_in, m_b_gate, m_rel_bias, m_w_attn_out, m_conv_dw_w, m_conv_dw_b, m_conv_ln_g, m_conv_ln_b, m_conv_pw_w, m_w_out, m_norm_mix_pre, m_norm_mix_post, m_norm_ffn_pre, m_norm_ffn_post, m_w_up, m_ffn_conv_w, m_ffn_conv_b, m_w_down, v_w_in, v_b_gate, v_rel_bias, v_w_attn_out, v_conv_dw_w, v_conv_dw_b, v_conv_ln_g, v_conv_ln_b, v_conv_pw_w, v_w_out, v_norm_mix_pre, v_norm_mix_post, v_norm_ffn_pre, v_norm_ffn_post, v_w_up, v_ffn_conv_w, v_ffn_conv_b, v_w_down):
    weights = dict(w_in=w_in, b_gate=b_gate, rel_bias=rel_bias, w_attn_out=w_attn_out, conv_dw_w=conv_dw_w, conv_dw_b=conv_dw_b,
                   conv_ln_g=conv_ln_g, conv_ln_b=conv_ln_b, conv_pw_w=conv_pw_w, w_out=w_out, norm_mix_pre=norm_mix_pre,
                   norm_mix_post=norm_mix_post, norm_ffn_pre=norm_ffn_pre, norm_ffn_post=norm_ffn_post, w_up=w_up,
                   ffn_conv_w=ffn_conv_w, ffn_conv_b=ffn_conv_b, w_down=w_down)
    m_in = dict(w_in=m_w_in, b_gate=m_b_gate, rel_bias=m_rel_bias, w_attn_out=m_w_attn_out, conv_dw_w=m_conv_dw_w,
                conv_dw_b=m_conv_dw_b, conv_ln_g=m_conv_ln_g, conv_ln_b=m_conv_ln_b, conv_pw_w=m_conv_pw_w, w_out=m_w_out,
                norm_mix_pre=m_norm_mix_pre, norm_mix_post=m_norm_mix_post, norm_ffn_pre=m_norm_ffn_pre,
                norm_ffn_post=m_norm_ffn_post, w_up=m_w_up, ffn_conv_w=m_ffn_conv_w, ffn_conv_b=m_ffn_conv_b, w_down=m_w_down)
    v_in = dict(w_in=v_w_in, b_gate=v_b_gate, rel_bias=v_rel_bias, w_attn_out=v_w_attn_out, conv_dw_w=v_conv_dw_w,
                conv_dw_b=v_conv_dw_b, conv_ln_g=v_conv_ln_g, conv_ln_b=v_conv_ln_b, conv_pw_w=v_conv_pw_w, w_out=v_w_out,
                norm_mix_pre=v_norm_mix_pre, norm_mix_post=v_norm_mix_post, norm_ffn_pre=v_norm_ffn_pre,
                norm_ffn_post=v_norm_ffn_post, w_up=v_w_up, ffn_conv_w=v_ffn_conv_w, ffn_conv_b=v_ffn_conv_b, w_down=v_w_down)
    names = list(weights)
    xi, yi, ci = _position()
    chip = 2 * xi + yi
    core_arr = jnp.reshape(ci, (1,)).astype(jnp.int32)

    xs = x[0]
    target = loss_target[0]
    S, D = xs.shape

    big = ["w_in", "w_attn_out", "conv_pw_w", "w_out", "w_up", "w_down"]
    row_sharded = ("conv_pw_w", "w_out", "w_down")
    natural = lambda k, g: g.reshape(-1, g.shape[2]) if k in row_sharded else g
    first_srcs = [w_in[0].astype(BF16), conv_dw_w[0], ffn_conv_w[0]]
    first_lands = [lax.empty((N_CHIPS,) + s.shape, s.dtype) for s in first_srcs]
    (first_hop,) = _split_start("gather_in_start", [(first_srcs, first_lands, 4 * len(first_srcs), _first_hop_copies)], core_arr)
    launched = first_hop["token"]
    late_sets = dict(mix=["w_attn_out", "conv_pw_w", "w_out"], up=["w_up"], down=["w_down"])
    late_groups = []
    for keys in late_sets.values():
        srcs = [(weights[k][0] + launched).astype(BF16) for k in keys]
        late_groups.append((srcs, [lax.empty((N_CHIPS,) + s.shape, BF16) for s in srcs], 4 * len(keys), _gather_copies))
    started = {}

    def late_weights(tag, after):
        if tag == "in":
            w_in_halves, dw4, fc4 = _split_wait("gather_in_wait", first_hop, _first_hop_copies, after)[len(first_srcs):]
            second_hop, *late = _split_start("gather_in_pass_start", [([], [w_in_halves], 3, _second_hop_copies)] + late_groups, dw4)
            started.update(zip(late_sets, late))
            (w_in_full,) = _split_wait("gather_in_pass_wait", second_hop, _second_hop_copies, second_hop["tile"])
            return dict(w_in=w_in_full, conv_dw_w=jnp.concatenate(list(dw4), axis=1), ffn_conv_w=jnp.concatenate(list(fc4), axis=1))
        landed = _split_wait(f"gather_{tag}_wait", started[tag], _gather_copies, after)[len(late_sets[tag]):]
        return {k: natural(k, g) for k, g in zip(late_sets[tag], landed)}

    chip_core = jnp.stack([chip, ci]).astype(jnp.int32)
    exchanging, pending, second_half = {}, {}, {}

    held = []

    def launch(tag, after, carry=None):
        keys, groups, partial = [], [], {}
        for k in list(exchanging):
            st, copies = exchanging.pop(k)
            gk, r1 = _split_wait(f"sibling_exchange_wait_{k}", st, copies, after)
            if k in second_half:
                partial[k], s16 = second_half.pop(k)(init=r1)
            else:
                partial[k], s16 = _pair_sum_call(gk, r1, core_arr, f"pair_sum_{k}")
            keys.append(k)
            groups.append(([s16], [lax.empty((3,) + s16.shape[1:], BF16)], 3, _exchange_copies))
        fresh = [(k, copies) for k, _, copies in held]
        for _, g3, copies in held:
            rows = g3.shape[1] // 2 if copies is _sibling_copies else g3.shape[1]
            groups.append(([g3], [lax.empty((N_CHIPS, rows, g3.shape[2]), F32)], 1, copies))
        held.clear()
        begun = _split_start(f"grad_exchange_start_{tag}", groups, core_arr, carry)
        for k, st in zip(keys, begun):
            pending[k] = (partial[k], st)
        for (k, copies), st in zip(fresh, begun[len(keys):]):
            exchanging[k] = (st, copies)
        return begun[0]["carry"]

    def on_grad(k, g, carry=None):
        if k is None:
            return launch("last", carry[:SUBLANES, :LANES], carry)
        if callable(g):
            theirs = g(1 - core_arr, carry)
            held.append((k, theirs, _sibling_whole_copies))
            carried = launch(k, theirs[0, :SUBLANES, :LANES], carry)
            second_half[k] = functools.partial(g, core_arr, carried)
            return carried
        held.append((k, g, _sibling_copies))
        if k in ("w_down", "w_out", "w_attn_out"):
            return carry
        return launch(k, g[0, :SUBLANES, :LANES], carry)

    def finish(keys, after, tag):
        halves = []
        for k in keys:
            s32, st = pending[k]
            recv2 = _split_wait(f"chip_exchange_wait_{k}", st, _exchange_copies, after)[1]
            halves.append(_chip_sum_call(s32, recv2, chip_core, f"chip_sum_{k}"))
        return dict(zip(keys, _sibling_assemble_call(halves, f"grad_sibling_assemble_{tag}")))

    P = dict(b_gate=b_gate, rel_bias=rel_bias, conv_dw_b=conv_dw_b, conv_ln_g=conv_ln_g, conv_ln_b=conv_ln_b,
             norm_mix_pre=norm_mix_pre + launched, norm_mix_post=norm_mix_post, norm_ffn_pre=norm_ffn_pre,
             norm_ffn_post=norm_ffn_post, ffn_conv_b=ffn_conv_b)
    loss_tile, grad_x, G = _local_step(xs, target, P, late_weights, on_grad)

    small = [k for k in names if k not in big]
    packed = _pack([loss_tile[:1]] + [G[k] for k in small])
    (allsum,) = _split_start("small_grad_allsum_start",
                             [([packed], [jnp.zeros((N_DEVICES,) + packed.shape, F32)], N_DEVICES - 1, _allsum_copies)], core_arr)

    reduced, grads, deltas, new_m, new_v = {}, {}, {}, {}, {}

    def update(keys):
        for k in keys:
            gk, d, mn, vn = _adamw_call(weights[k][0], reduced[k], m_in[k][0], v_in[k][0], f"adamw_{k}")
            grads[k], deltas[k], new_m[k], new_v[k] = gk[None], d[None], mn[None], vn[None]

    others = [k for k in big if k != "w_in"]
    reduced.update(finish(others, allsum["tile"], "others"))
    update(others)
    reduced.update(finish(["w_in"], deltas["w_up"], "w_in"))
    update(["w_in"])

    me_chip = jnp.stack([4 * xi + 2 * yi + ci, chip]).astype(jnp.int32)
    mine, landed = _split_wait("small_grad_allsum_wait", allsum, _allsum_copies, deltas["w_in"])
    piece_shapes = [(1, LANES)] + [(G[k].size // LANES, LANES) if k == "rel_bias" else G[k].shape for k in small]
    piece_cols = [0] + [weights[k].shape[2] if k in ("conv_dw_w", "ffn_conv_w") else 0 for k in small]
    loss_row, *summed = _ordered_sum_call(mine, landed, me_chip, piece_shapes, piece_cols)
    loss = loss_row[0, 0]
    for k, gsum in zip(small, summed):
        grads[k] = gsum.reshape(weights[k].shape)
    ds, mns, vns = _adamw_small_call([weights[k] for k in small], [grads[k] for k in small],
                                     [m_in[k] for k in small], [v_in[k] for k in small])
    deltas.update(zip(small, ds))
    new_m.update(zip(small, mns))
    new_v.update(zip(small, vns))

    return (loss, grad_x[None], *[grads[k] for k in names], *[deltas[k] for k in names],
            *[new_m[k] for k in names], *[new_v[k] for k in names])
```

```python
import functools
import math

import jax
import jax.numpy as jnp
import numpy as np
from jax import lax
from jax.experimental import pallas as pl
from jax.experimental.pallas import tpu as pltpu

F32 = jnp.float32
BF16 = jnp.bfloat16
MESH = pl.DeviceIdType.MESH

HEAD_DIM = 128
HEADS_PER_GROUP = 4
DILATED_PATTERNS = ((128, 1), (512, 4), (2048, 16))
N_GROUPS = 3
N_HEADS = N_GROUPS * HEADS_PER_GROUP
SPAN = 128
GROUP_WIDTH = HEADS_PER_GROUP * HEAD_DIM
CONV_WIDTH = 31
FFN_CONV_WIDTH = 3
N_BUCKETS = 32
MAX_DISTANCE = 2048
RMS_EPS = 1e-6
LN_EPS = 1e-5
NEG_INF = -1e30
ADAM_LR = 0.001
ADAM_B1 = 0.9
ADAM_B2 = 0.999
ADAM_EPS = 1e-08
ADAM_WD = 0.01
ADAM_STEP = 10

LANES = 128
SUBLANES = 8
PACKED_ROWS = 16
ROW_TILE = 512
GATE_ROWS, GATE_COLS = 512, 512
TIME_BLOCK = 128
CONV_PAD = 32
FFN_PAD = 8
VMEM_LIMIT = 56 << 20


def _params(sem=None, vmem=None):
    kw = {}
    if sem is not None:
        kw["dimension_semantics"] = sem
    if vmem is not None:
        kw["vmem_limit_bytes"] = vmem
    return pltpu.CompilerParams(**kw)


def _pick(n, cands):
    for c in cands:
        if n % c == 0:
            return c
    return n


ELEMENTWISE_TILE_BYTES = 3 << 19


def _row_tile(rows, cols):
    for align in (16, SUBLANES):
        fits = [t for t in range(align, rows + 1, align) if rows % t == 0 and t * cols * 4 <= ELEMENTWISE_TILE_BYTES]
        if fits:
            return max(fits)
    return SUBLANES


N_CHIPS = 4
M_TILES = (1024, 1408, 512, 256, 128)
N_TILES = (1024, 512, 1408, 256, 128)
K_TILES = (2176, 2048, 1408, 1024, 512, 256, 128)


def _matmul(a, b, mode, name, out_shards=False, tm=None):
    assert a.dtype == BF16 and b.dtype == BF16, (name, a.dtype, b.dtype)
    b3 = b.ndim == 3
    tn = tk = None
    halves = None
    if mode == "nn":
        M, K = a.shape
        N = b.shape[-1] * (N_CHIPS if b3 else 1)
        tn = b.shape[-1] if b3 else None
    elif mode == "nt":
        if a.ndim == 3:
            halves = a.shape[2]
        M, K = a.shape[-2], a.shape[-1] * (a.shape[0] if a.ndim == 3 else 1)
        N = b.shape[-2]
        tk = b.shape[-1] if b3 else None
    else:
        if b3:
            halves = b.shape[2]
        K, M = a.shape
        N = b.shape[-1] * (b.shape[0] if b3 else 1)
        tn = N // N_CHIPS if out_shards else None
    tm = tm or _pick(M, M_TILES)
    tn = tn or _pick(N, N_TILES)
    tk = tk or _pick(K, K_TILES)
    nk = K // tk
    dn = {"nn": (((1,), (0,)), ((), ())), "nt": (((1,), (1,)), ((), ())), "tn": (((0,), (0,)), ((), ()))}[mode]

    def body(a_ref, b_ref, o_ref):
        if nk == 1:
            o_ref[...] = lax.dot_general(a_ref[...], b_ref[...], dn, preferred_element_type=F32)
        else:
            @pl.when(pl.program_id(2) == 0)
            def _():
                o_ref[...] = jnp.zeros_like(o_ref)

            o_ref[...] += lax.dot_general(a_ref[...], b_ref[...], dn, preferred_element_type=F32)

    if mode == "tn":
        a_spec = pl.BlockSpec((tk, tm), lambda i, j, k: (k, i))
    elif halves:
        per = halves // tk
        a_spec = pl.BlockSpec((None, tm, tk), lambda i, j, k: (k // per, i, k % per))
    else:
        a_spec = pl.BlockSpec((tm, tk), lambda i, j, k: (i, k))
    if mode == "nn":
        b_spec = pl.BlockSpec((None, tk, tn), lambda i, j, k: (j, k, 0)) if b3 else pl.BlockSpec((tk, tn), lambda i, j, k: (k, j))
    elif mode == "nt":
        b_spec = pl.BlockSpec((None, tn, tk), lambda i, j, k: (k, j, 0)) if b3 else pl.BlockSpec((tn, tk), lambda i, j, k: (j, k))
    elif halves:
        per = halves // tn
        b_spec = pl.BlockSpec((None, tk, tn), lambda i, j, k: (j // per, k, j % per))
    else:
        b_spec = pl.BlockSpec((tk, tn), lambda i, j, k: (k, j))
    if out_shards:
        out_spec = pl.BlockSpec((None, tm, tn), lambda i, j, k: (j, i, 0))
        out_shape = jax.ShapeDtypeStruct((N_CHIPS, M, tn), F32)
    else:
        out_spec = pl.BlockSpec((tm, tn), lambda i, j, k: (i, j))
        out_shape = jax.ShapeDtypeStruct((M, N), F32)
    return pl.pallas_call(
        body, name=name, grid=(M // tm, N // tn, nk),
        in_specs=[a_spec, b_spec], out_specs=out_spec, out_shape=out_shape,
        compiler_params=_params(("parallel", "parallel", "arbitrary"), VMEM_LIMIT),
    )(a, b)


GRAD_K_TILE = 512


def _grad_half_matmul(a, name, chip_half, b, init=None):
    K, M = a.shape
    parts = b.ndim == 3
    N = b.shape[-1] * (b.shape[0] if parts else 1)
    h, tn = M // 2, N // N_CHIPS
    tk = GRAD_K_TILE
    assert K % tk == 0, (name, K)
    nk = K // tk
    summed = init is not None
    dn = (((0,), (0,)), ((), ()))

    def body(ch_ref, a_ref, b_ref, *rest):
        j, k = pl.program_id(0), pl.program_id(1)
        acc_ref = rest[-1]
        product = lax.dot_general(a_ref[...], b_ref[...], dn, preferred_element_type=F32)

        @pl.when(k == 0)
        def _():
            acc_ref[...] = product + rest[0][...] if summed else product

        @pl.when(k > 0)
        def _():
            acc_ref[...] += product

        if summed:
            @pl.when(k == nk - 1)
            def _():
                rest[2][...] = acc_ref[...].astype(BF16)

            @pl.when((k == nk - 1) & (j == ch_ref[0]))
            def _():
                rest[1][...] = acc_ref[...]

    if parts:
        per = b.shape[2] // tn
        b_spec = pl.BlockSpec((None, tk, tn), lambda j, k, ch_ref: (j // per, k, j % per))
    else:
        b_spec = pl.BlockSpec((tk, tn), lambda j, k, ch_ref: (k, j))
    shard_spec = pl.BlockSpec((None, h, tn), lambda j, k, ch_ref: (j, 0, 0))
    own_spec = pl.BlockSpec((h, tn), lambda j, k, ch_ref: (0, 0))
    shape = (N_CHIPS, h, tn)
    return pl.pallas_call(
        body, name=name + ("_mine" if summed else "_theirs"),
        grid_spec=pltpu.PrefetchScalarGridSpec(
            num_scalar_prefetch=1, grid=(N_CHIPS, nk),
            in_specs=[pl.BlockSpec((tk, h), lambda j, k, ch_ref: (k, ch_ref[1])), b_spec] + [shard_spec] * summed,
            out_specs=[own_spec, shard_spec] if summed else shard_spec,
            scratch_shapes=[pltpu.VMEM((h, tn), F32)] * summed),
        out_shape=[jax.ShapeDtypeStruct((h, tn), F32), jax.ShapeDtypeStruct(shape, BF16)] if summed
        else jax.ShapeDtypeStruct(shape, F32),
        compiler_params=_params(("arbitrary", "arbitrary"), VMEM_LIMIT),
    )(chip_half, a, b, *([init] if summed else []))


def _rms(x, g):
    r = lax.rsqrt(jnp.mean(x * x, axis=-1, keepdims=True) + RMS_EPS)
    return x * r * g


def _rms_bwd(x, g, dy):
    r = lax.rsqrt(jnp.mean(x * x, axis=-1, keepdims=True) + RMS_EPS)
    n = x * r
    dn = dy * g
    dx = r * (dn - n * jnp.mean(dn * n, axis=-1, keepdims=True))
    return dx, jnp.sum(dy * n, axis=0, keepdims=True)


def _sigmoid(x):
    return 1.0 / (1.0 + jnp.exp(-x))


_GELU_C = math.sqrt(2.0 / math.pi)


def _gelu(x):
    return 0.5 * x * (1.0 + jnp.tanh(_GELU_C * (x + 0.044715 * x * x * x)))


def _gelu_and_grad(x):
    x2 = x * x
    t = jnp.tanh(_GELU_C * x * (1.0 + 0.044715 * x2))
    half = 0.5 * (1.0 + t)
    return x * half, half + (0.5 * _GELU_C) * x * (1.0 - t * t) * (1.0 + (3.0 * 0.044715) * x2)


def _row_spec(width, col_block=0):
    return pl.BlockSpec((ROW_TILE, width), lambda i: (i, col_block))


def _vec_spec(width, col_block=0):
    return pl.BlockSpec((1, width), lambda i: (0, col_block))


def _accumulate(ref, part):
    @pl.when(pl.program_id(0) == 0)
    def _():
        ref[...] = part

    @pl.when(pl.program_id(0) > 0)
    def _():
        ref[...] += part


def _rms_fwd_call(x, g):
    S, D = x.shape

    def body(x_ref, g_ref, h_ref):
        h_ref[...] = _rms(x_ref[...], g_ref[...]).astype(BF16)

    return pl.pallas_call(
        body, name="rms_mix_pre", grid=(S // ROW_TILE,),
        in_specs=[_row_spec(D), _vec_spec(D)], out_specs=_row_spec(D),
        out_shape=jax.ShapeDtypeStruct((S, D), BF16),
        compiler_params=_params(("parallel",)),
    )(x, g)


def _ln_silu_call(c1, g, b):
    S, C = c1.shape

    def body(c_ref, g_ref, b_ref, o_ref):
        xv = c_ref[...]
        mu = jnp.mean(xv, axis=-1, keepdims=True)
        xc = xv - mu
        var = jnp.mean(xc * xc, axis=-1, keepdims=True)
        z = xc * lax.rsqrt(var + LN_EPS) * g_ref[...] + b_ref[...]
        o_ref[...] = (z * _sigmoid(z)).astype(BF16)

    return pl.pallas_call(
        body, name="conv_ln_silu", grid=(S // ROW_TILE,),
        in_specs=[_row_spec(C), _vec_spec(C), _vec_spec(C)], out_specs=_row_spec(C),
        out_shape=jax.ShapeDtypeStruct((S, C), BF16),
        compiler_params=_params(("parallel",)),
    )(c1, g, b)


def _ln_silu_bwd_call(c1, g, b, dc):
    S, C = c1.shape

    def body(c_ref, g_ref, b_ref, dc_ref, dx_ref, dg_ref, db_ref):
        xv = c_ref[...]
        mu = jnp.mean(xv, axis=-1, keepdims=True)
        xc = xv - mu
        rs = lax.rsqrt(jnp.mean(xc * xc, axis=-1, keepdims=True) + LN_EPS)
        xh = xc * rs
        z = xh * g_ref[...] + b_ref[...]
        sg = _sigmoid(z)
        dz = dc_ref[...] * (sg * (1.0 + z * (1.0 - sg)))
        dxh = dz * g_ref[...]
        dx_ref[...] = rs * (dxh - jnp.mean(dxh, axis=-1, keepdims=True) - xh * jnp.mean(dxh * xh, axis=-1, keepdims=True))
        _accumulate(dg_ref, jnp.sum(dz * xh, axis=0, keepdims=True))
        _accumulate(db_ref, jnp.sum(dz, axis=0, keepdims=True))

    return pl.pallas_call(
        body, name="conv_ln_silu_bwd", grid=(S // ROW_TILE,),
        in_specs=[_row_spec(C), _vec_spec(C), _vec_spec(C), _row_spec(C)],
        out_specs=[_row_spec(C), _vec_spec(C), _vec_spec(C)],
        out_shape=[jax.ShapeDtypeStruct((S, C), F32), jax.ShapeDtypeStruct((1, C), F32), jax.ShapeDtypeStruct((1, C), F32)],
        compiler_params=_params(("arbitrary",)),
    )(c1, g, b, dc)


def _mix_call(proj, gate_col0, b_gate, y_a, y_c):
    S, D = y_a.shape
    w = GATE_COLS
    nc = D // w
    ga0, gc0 = gate_col0 // w, (gate_col0 + D) // w

    def body(ga_ref, gc_ref, ba_ref, bc_ref, ya_ref, yc_ref, o_ref):
        o_ref[...] = (_sigmoid(ga_ref[...] + ba_ref[...]) * ya_ref[...]
                      + _sigmoid(gc_ref[...] + bc_ref[...]) * yc_ref[...]).astype(BF16)

    tile = lambda off: pl.BlockSpec((GATE_ROWS, w), lambda i, j: (i, off + j))
    vec = lambda off: pl.BlockSpec((1, w), lambda i, j: (0, off + j))
    return pl.pallas_call(
        body, name="gate_mix", grid=(S // GATE_ROWS, nc),
        in_specs=[tile(ga0), tile(gc0), vec(0), vec(nc), tile(0), tile(0)],
        out_specs=tile(0), out_shape=jax.ShapeDtypeStruct((S, D), BF16),
        compiler_params=_params(("parallel", "parallel")),
    )(proj, proj, b_gate, b_gate, y_a, y_c)


def _mix_bwd_call(dmixed, proj, gate_col0, b_gate, y_a, y_c):
    S, D = y_a.shape
    w = GATE_COLS
    nc = D // w
    ga0, gc0 = gate_col0 // w, (gate_col0 + D) // w

    def body(dm_ref, ga_ref, gc_ref, ba_ref, bc_ref, ya_ref, yc_ref, dya_ref, dyc_ref, dga_ref, dgc_ref, dba_ref, dbc_ref):
        dm = dm_ref[...]
        sa = _sigmoid(ga_ref[...] + ba_ref[...])
        sc = _sigmoid(gc_ref[...] + bc_ref[...])
        dya_ref[...] = (dm * sa).astype(BF16)
        dyc_ref[...] = (dm * sc).astype(BF16)
        dga = dm * ya_ref[...] * sa * (1.0 - sa)
        dgc = dm * yc_ref[...] * sc * (1.0 - sc)
        dga_ref[...] = dga.astype(BF16)
        dgc_ref[...] = dgc.astype(BF16)
        pa = jnp.sum(dga, axis=0, keepdims=True)
        pc = jnp.sum(dgc, axis=0, keepdims=True)

        @pl.when(pl.program_id(1) == 0)
        def _():
            dba_ref[...] = pa
            dbc_ref[...] = pc

        @pl.when(pl.program_id(1) > 0)
        def _():
            dba_ref[...] += pa
            dbc_ref[...] += pc

    tile = lambda off: pl.BlockSpec((GATE_ROWS, w), lambda j, i: (i, off + j))
    vec = lambda off: pl.BlockSpec((1, w), lambda j, i: (0, off + j))
    return pl.pallas_call(
        body, name="gate_mix_bwd", grid=(nc, S // GATE_ROWS),
        in_specs=[tile(0), tile(ga0), tile(gc0), vec(0), vec(nc), tile(0), tile(0)],
        out_specs=[tile(0), tile(0), tile(0), tile(0), vec(0), vec(0)],
        out_shape=[jax.ShapeDtypeStruct((S, D), BF16)] * 4 + [
                   jax.ShapeDtypeStruct((1, D), F32), jax.ShapeDtypeStruct((1, D), F32)],
        compiler_params=_params(("parallel", "arbitrary")),
    )(dmixed, proj, proj, b_gate, b_gate, y_a, y_c)


def _res1_call(x, out, g_post, g_pre):
    S, D = x.shape

    def body(x_ref, o_ref, gp_ref, gq_ref, x1_ref, h2_ref):
        x1 = x_ref[...] + _rms(o_ref[...], gp_ref[...])
        x1_ref[...] = x1
        h2_ref[...] = _rms(x1, gq_ref[...]).astype(BF16)

    return pl.pallas_call(
        body, name="residual_mix", grid=(S // ROW_TILE,),
        in_specs=[_row_spec(D), _row_spec(D), _vec_spec(D), _vec_spec(D)],
        out_specs=[_row_spec(D), _row_spec(D)],
        out_shape=[jax.ShapeDtypeStruct((S, D), F32), jax.ShapeDtypeStruct((S, D), BF16)],
        compiler_params=_params(("parallel",)),
    )(x, out, g_post, g_pre)


def _loss_call(y, x1, g_post, target):
    S, D = y.shape

    def body(y_ref, x1_ref, g_ref, t_ref, loss_ref, dx_ref, dy_ref, dg_ref):
        yv, gv = y_ref[...], g_ref[...]
        err = x1_ref[...] + _rms(yv, gv) - t_ref[...]
        dx2 = err * (1.0 / D)
        dx_ref[...] = dx2
        dy, dg = _rms_bwd(yv, gv, dx2)
        dy_ref[...] = dy.astype(BF16)
        _accumulate(dg_ref, dg)
        part = 0.5 * jnp.sum(jnp.mean(err * err, axis=-1, keepdims=True), axis=0, keepdims=True)
        _accumulate(loss_ref, jnp.broadcast_to(part, (SUBLANES, LANES)))

    return pl.pallas_call(
        body, name="residual_ffn_loss", grid=(S // ROW_TILE,),
        in_specs=[_row_spec(D), _row_spec(D), _vec_spec(D), _row_spec(D)],
        out_specs=[pl.BlockSpec((SUBLANES, LANES), lambda i: (0, 0)), _row_spec(D), _row_spec(D), _vec_spec(D)],
        out_shape=[jax.ShapeDtypeStruct((SUBLANES, LANES), F32), jax.ShapeDtypeStruct((S, D), F32),
                   jax.ShapeDtypeStruct((S, D), BF16), jax.ShapeDtypeStruct((1, D), F32)],
        compiler_params=_params(("arbitrary",)),
    )(y, x1, g_post, target)


def _mid_bwd_call(x1, g_pre, dh2, dx2, out, g_post):
    S, D = x1.shape

    def body(x1_ref, gq_ref, dh_ref, dx2_ref, o_ref, gp_ref, dx1_ref, do_ref, dgq_ref, dgp_ref):
        d, dgq = _rms_bwd(x1_ref[...], gq_ref[...], dh_ref[...])
        dx1 = dx2_ref[...] + d
        dx1_ref[...] = dx1
        do, dgp = _rms_bwd(o_ref[...], gp_ref[...], dx1)
        do_ref[...] = do.astype(BF16)
        _accumulate(dgq_ref, dgq)
        _accumulate(dgp_ref, dgp)

    return pl.pallas_call(
        body, name="residual_mix_bwd", grid=(S // ROW_TILE,),
        in_specs=[_row_spec(D), _vec_spec(D), _row_spec(D), _row_spec(D), _row_spec(D), _vec_spec(D)],
        out_specs=[_row_spec(D), _row_spec(D), _vec_spec(D), _vec_spec(D)],
        out_shape=[jax.ShapeDtypeStruct((S, D), F32), jax.ShapeDtypeStruct((S, D), BF16)] + [jax.ShapeDtypeStruct((1, D), F32)] * 2,
        compiler_params=_params(("arbitrary",)),
    )(x1, g_pre, dh2, dx2, out, g_post)


def _in_bwd_call(x, g, dh1, dx1):
    S, D = x.shape

    def body(x_ref, g_ref, dh_ref, dx1_ref, gx_ref, dg_ref):
        d, dg = _rms_bwd(x_ref[...], g_ref[...], dh_ref[...])
        gx_ref[...] = dx1_ref[...] + d
        _accumulate(dg_ref, dg)

    return pl.pallas_call(
        body, name="rms_mix_pre_bwd", grid=(S // ROW_TILE,),
        in_specs=[_row_spec(D), _vec_spec(D), _row_spec(D), _row_spec(D)],
        out_specs=[_row_spec(D), _vec_spec(D)],
        out_shape=[jax.ShapeDtypeStruct((S, D), F32), jax.ShapeDtypeStruct((1, D), F32)],
        compiler_params=_params(("arbitrary",)),
    )(x, g, dh1, dx1)


def _bucket_table(dilation):
    qi = np.arange(SPAN)[:, None]
    ki = np.arange(2 * SPAN)[None, :]
    dist = np.maximum(qi + SPAN - ki, 0) * dilation
    max_exact = N_BUCKETS // 2
    d = np.maximum(dist, 1).astype(np.float64)
    large = max_exact + (np.log(d / max_exact) / math.log(MAX_DISTANCE / max_exact) * (N_BUCKETS - max_exact)).astype(np.int32)
    large = np.minimum(large, N_BUCKETS - 1)
    return np.where(dist < max_exact, dist, large).astype(np.int32)


def _bucket_tables():
    return jnp.asarray(np.stack([_bucket_table(r) for _, r in DILATED_PATTERNS]))


def _bias_table_call(rel_bias, buckets):
    def body(rb_ref, bk_ref, o_ref):
        for h in range(N_HEADS):
            bk = bk_ref[h // HEADS_PER_GROUP]

            def step(b, acc):
                return jnp.where(bk == b, rb_ref[b, h], acc)

            o_ref[h] = lax.fori_loop(0, N_BUCKETS, step, jnp.zeros((SPAN, 2 * SPAN), F32))

    return pl.pallas_call(
        body, name="rel_bias_table",
        in_specs=[pl.BlockSpec(memory_space=pltpu.SMEM), pl.BlockSpec(memory_space=pltpu.VMEM)],
        out_specs=pl.BlockSpec(memory_space=pltpu.VMEM),
        out_shape=jax.ShapeDtypeStruct((N_HEADS, SPAN, 2 * SPAN), F32),
    )(rel_bias, buckets)


def _bias_grad_call(dbias, buckets):
    def body(db_ref, bk_ref, o_ref, rows_ref):
        for h in range(N_HEADS):
            bk = bk_ref[h // HEADS_PER_GROUP]
            dv = db_ref[h]

            def step(b, carry):
                rows_ref[h, b] = jnp.sum(jnp.where(bk == b, dv, 0.0), axis=0, keepdims=True)
                return carry

            lax.fori_loop(0, N_BUCKETS, step, 0)
        o_ref[...] = jnp.sum(rows_ref[...], axis=-1, keepdims=True)

    out = pl.pallas_call(
        body, name="rel_bias_grad",
        in_specs=[pl.BlockSpec(memory_space=pltpu.VMEM), pl.BlockSpec(memory_space=pltpu.VMEM)],
        out_specs=pl.BlockSpec(memory_space=pltpu.VMEM),
        out_shape=jax.ShapeDtypeStruct((N_HEADS, N_BUCKETS, 1, 1), F32),
        scratch_shapes=[pltpu.VMEM((N_HEADS, N_BUCKETS, 1, 2 * SPAN), F32)],
    )(dbias, buckets)
    return out.reshape(N_HEADS, N_BUCKETS).T


def _dot_nt(a, b):
    return lax.dot_general(a, b, (((1,), (1,)), ((), ())), preferred_element_type=F32)


def _dot_nn(a, b):
    return lax.dot_general(a, b, (((1,), (0,)), ((), ())), preferred_element_type=F32)


def _dot_tn(a, b):
    return lax.dot_general(a, b, (((0,), (0,)), ((), ())), preferred_element_type=F32)


def _band_masks(n, nb):
    qi = lax.broadcasted_iota(jnp.int32, (SPAN, SPAN), 0)
    ki = lax.broadcasted_iota(jnp.int32, (SPAN, SPAN), 1)
    prev_ok = jnp.logical_and(ki >= qi, n > 0)
    cur_ok = ki <= qi
    next_ok = jnp.logical_and(ki >= qi, n < nb - 1)
    return prev_ok, cur_ok, next_ok


def _wide_band_mask(n):
    qi = lax.broadcasted_iota(jnp.int32, (SPAN, 2 * SPAN), 0)
    ki = lax.broadcasted_iota(jnp.int32, (SPAN, 2 * SPAN), 1)
    prev_ok = jnp.logical_and(jnp.logical_and(ki < SPAN, ki >= qi), n > 0)
    cur_ok = jnp.logical_and(ki >= SPAN, ki - SPAN <= qi)
    return jnp.logical_or(prev_ok, cur_ok)


def _attn_plan(S, group):
    r = DILATED_PATTERNS[group][1]
    hp, per = (HEADS_PER_GROUP, 1) if r == 1 else (2, 4)
    return r, S // (r * SPAN), hp, per


def _residue_rows(rho, r):
    return slice(None) if r == 1 else pl.ds(rho, SPAN, stride=r)


def _for_residues(r, per, fn):
    if r == per:
        for u in range(per):
            fn(u)
        return

    def step(i, carry):
        for u in range(per):
            fn(i * per + u)
        return carry

    lax.fori_loop(0, r // per, step, 0)


def _attn_fwd_call(proj, bias, group):
    S = proj.shape[0]
    r, nb, hp, per = _attn_plan(S, group)
    scale = HEAD_DIM ** -0.5
    kinds = ("q", "kp", "kc", "vp", "vc") if nb > 1 else ("q", "kc", "vc")

    per_kind = _refs_per_kind(r, hp)

    def body(*refs):
        ins = {kind: refs[i * per_kind:(i + 1) * per_kind] for i, kind in enumerate(kinds)}
        b_ref, o_ref, lse_ref = refs[len(kinds) * per_kind:]
        n = pl.program_id(1)
        prev_ok, cur_ok, _ = _band_masks(n, nb)

        band_ok = _wide_band_mask(n) if nb > 1 else cur_ok

        def residue(rho):
            rows = _residue_rows(rho, r)
            for j in range(hp):
                get = lambda kind: _head_rows(ins[kind], j, rows, r).astype(BF16)
                q = get("q")
                if nb > 1:
                    keys, vals, bias_j = jnp.concatenate([get("kp"), get("kc")], axis=0), jnp.concatenate([get("vp"), get("vc")], axis=0), b_ref[j]
                else:
                    keys, vals, bias_j = get("kc"), get("vc"), b_ref[j, :, SPAN:]
                s = jnp.where(band_ok, _dot_nt(q, keys) * scale + bias_j, NEG_INF)
                m = jnp.max(s, axis=-1, keepdims=True)
                p = jnp.exp(s - m)
                den = jnp.sum(p, axis=-1, keepdims=True)
                o_ref[j, rows, :] = _dot_nn(p.astype(BF16), vals) / den
                lse_ref[j, rows, :] = jnp.broadcast_to(m + jnp.log(den), (SPAN, HEAD_DIM))

        _for_residues(r, per, residue)

    in_specs = [_head_spec(r, nb, hp, kind, group, jj) for kind in kinds for jj in range(per_kind)]
    in_specs.append(pl.BlockSpec((hp, SPAN, 2 * SPAN), lambda j, n: (group * (HEADS_PER_GROUP // hp) + j, 0, 0)))
    out = pl.BlockSpec((hp, r * SPAN, HEAD_DIM), lambda j, n: (j, n, 0))
    return pl.pallas_call(
        body, name=f"attn_fwd_g{group}", grid=(HEADS_PER_GROUP // hp, nb),
        in_specs=in_specs, out_specs=[out] * 2,
        out_shape=[jax.ShapeDtypeStruct((HEADS_PER_GROUP, S, HEAD_DIM), F32)] * 2,
        compiler_params=_params(("parallel", "parallel"), VMEM_LIMIT),
    )(*([proj] * (len(in_specs) - 1)), bias)


_PROJ_PART = dict(q=0, qn=0, kp=1, kc=1, vp=2, vc=2)


def _refs_per_kind(r, hp):
    return 1 if r == 1 else hp


def _head_rows(refs, j, rows, r):
    return refs[0][:, j * HEAD_DIM:(j + 1) * HEAD_DIM] if r == 1 else refs[j][rows, :]


def _head_spec(r, nb, hp, kind, group, jj):
    if kind in _PROJ_PART:
        base = (_PROJ_PART[kind] * N_GROUPS + group) * HEADS_PER_GROUP
    else:
        base = 0
    if kind.endswith("p"):
        row = lambda n: jnp.maximum(n - 1, 0)
    elif kind.endswith("n"):
        row = lambda n: jnp.minimum(n + 1, nb - 1)
    else:
        row = lambda n: n
    if r == 1:
        return pl.BlockSpec((SPAN, hp * HEAD_DIM), lambda j, n: (row(n), base // hp + j))
    return pl.BlockSpec((r * SPAN, HEAD_DIM), lambda j, n: (row(n), base + j * hp + jj))


def _attn_merge_call(parts):
    S = parts[0].shape[1]

    def body(o1, s1, o2, s2, o3, s3, a_ref, ab_ref, lse_ref):
        for j in range(HEADS_PER_GROUP):
            sl = slice(j * HEAD_DIM, (j + 1) * HEAD_DIM)
            mx = jnp.maximum(jnp.maximum(s1[j], s2[j]), s3[j])
            w1 = jnp.exp(s1[j] - mx)
            w2 = jnp.exp(s2[j] - mx)
            w3 = jnp.exp(s3[j] - mx)
            den = w1 + w2 + w3
            a = (w1 * o1[j] + w2 * o2[j] + w3 * o3[j]) / den
            a_ref[:, sl] = a
            ab_ref[:, sl] = a.astype(BF16)
            lse_ref[:, sl] = mx + jnp.log(den)

    heads = pl.BlockSpec((HEADS_PER_GROUP, ROW_TILE, HEAD_DIM), lambda i: (0, i, 0))
    return pl.pallas_call(
        body, name="attn_merge", grid=(S // ROW_TILE,),
        in_specs=[heads] * 6, out_specs=[_row_spec(GROUP_WIDTH)] * 3,
        out_shape=[jax.ShapeDtypeStruct((S, GROUP_WIDTH), F32), jax.ShapeDtypeStruct((S, GROUP_WIDTH), BF16),
                   jax.ShapeDtypeStruct((S, GROUP_WIDTH), F32)],
        compiler_params=_params(("parallel",)),
    )(*parts)


def _attn_delta_call(a, da):
    S = a.shape[0]

    def body(a_ref, da_ref, d_ref):
        for j in range(HEADS_PER_GROUP):
            sl = slice(j * HEAD_DIM, (j + 1) * HEAD_DIM)
            d = jnp.sum(a_ref[:, sl] * da_ref[:, sl], axis=-1, keepdims=True)
            d_ref[:, sl] = jnp.broadcast_to(d, (ROW_TILE, HEAD_DIM))

    return pl.pallas_call(
        body, name="attn_delta", grid=(S // ROW_TILE,),
        in_specs=[_row_spec(GROUP_WIDTH)] * 2, out_specs=_row_spec(GROUP_WIDTH),
        out_shape=jax.ShapeDtypeStruct((S, GROUP_WIDTH), F32),
        compiler_params=_params(("parallel",)),
    )(a, da)


def _attn_bwd_call(proj, bias, da, lse, delta, group):
    S = proj.shape[0]
    r, nb, hp, per = _attn_plan(S, group)
    scale = HEAD_DIM ** -0.5
    kinds = ("q", "qn", "kp", "kc", "vp", "vc", "da", "dan", "lse", "lsen", "dl", "dln") if nb > 1 else ("q", "kc", "vc", "da", "lse", "dl")
    source = dict(da=da, dan=da, lse=lse, lsen=lse, dl=delta, dln=delta)

    per_kind = _refs_per_kind(r, hp)

    def body(*refs):
        ins = {kind: refs[i * per_kind:(i + 1) * per_kind] for i, kind in enumerate(kinds)}
        b_ref, dq_ref, dk_ref, dv_ref, db_ref = refs[len(kinds) * per_kind:]
        n = pl.program_id(1)
        prev_ok, cur_ok, next_ok = _band_masks(n, nb)

        @pl.when(n == 0)
        def _():
            db_ref[...] = jnp.zeros_like(db_ref)

        band_ok = _wide_band_mask(n) if nb > 1 else cur_ok

        def residue(rho):
            rows = _residue_rows(rho, r)
            for j in range(hp):
                get = lambda kind: _head_rows(ins[kind], j, rows, r)
                q = get("q").astype(BF16)
                kc = get("kc").astype(BF16)
                vc = get("vc").astype(BF16)
                dav = get("da").astype(BF16)
                lse_q, dl_q = get("lse"), get("dl")
                if nb == 1:
                    pc = jnp.exp(jnp.where(cur_ok, _dot_nt(q, kc) * scale + b_ref[j, :, SPAN:], NEG_INF) - lse_q)
                    dsc = pc * (_dot_nt(dav, vc) - dl_q)
                    dsc_b = dsc.astype(BF16)
                    dq = _dot_nn(dsc_b, kc)
                    dk = _dot_tn(dsc_b, q)
                    dv = _dot_tn(pc.astype(BF16), dav)
                    db_ref[j, :, SPAN:] += dsc
                else:
                    qn = get("qn").astype(BF16)
                    dan = get("dan").astype(BF16)
                    keys = jnp.concatenate([get("kp").astype(BF16), kc], axis=0)
                    vals = jnp.concatenate([get("vp").astype(BF16), vc], axis=0)
                    wide = lambda t: jnp.concatenate([t, t], axis=1)
                    p = jnp.exp(jnp.where(band_ok, _dot_nt(q, keys) * scale + b_ref[j], NEG_INF) - wide(lse_q))
                    ds = p * (_dot_nt(dav, vals) - wide(dl_q))
                    dq = _dot_nn(ds.astype(BF16), keys)
                    db_ref[j] += ds
                    pn = jnp.exp(jnp.where(next_ok, _dot_nt(qn, kc) * scale + b_ref[j, :, :SPAN], NEG_INF) - get("lsen"))
                    dsn = pn * (_dot_nt(dan, vc) - get("dln"))
                    both = lambda cur_part, next_part: jnp.concatenate([cur_part.astype(BF16), next_part.astype(BF16)], axis=0)
                    dk = _dot_tn(both(ds[:, SPAN:], dsn), jnp.concatenate([q, qn], axis=0))
                    dv = _dot_tn(both(p[:, SPAN:], pn), jnp.concatenate([dav, dan], axis=0))
                dq_ref[j, rows, :] = dq * scale
                dk_ref[j, rows, :] = dk * scale
                dv_ref[j, rows, :] = dv

        _for_residues(r, per, residue)

    per_group = HEADS_PER_GROUP // hp
    band = (hp, SPAN, 2 * SPAN)
    in_specs = [_head_spec(r, nb, hp, kind, group, jj) for kind in kinds for jj in range(per_kind)]
    in_specs.append(pl.BlockSpec(band, lambda j, n: (group * per_group + j, 0, 0)))
    operands = [source.get(kind, proj) for kind in kinds for _ in range(per_kind)] + [bias]
    out = pl.BlockSpec((hp, r * SPAN, HEAD_DIM), lambda j, n: (j, n, 0))
    return pl.pallas_call(
        body, name=f"attn_bwd_g{group}", grid=(per_group, nb),
        in_specs=in_specs,
        out_specs=[out] * 3 + [pl.BlockSpec(band, lambda j, n: (j, 0, 0))],
        out_shape=[jax.ShapeDtypeStruct((HEADS_PER_GROUP, S, HEAD_DIM), F32)] * 3
        + [jax.ShapeDtypeStruct((HEADS_PER_GROUP, SPAN, 2 * SPAN), F32)],
        compiler_params=_params(("parallel", "arbitrary"), VMEM_LIMIT),
    )(*operands)


def _dproj_call(dqkv, tails):
    S = tails[0].shape[0]
    width = len(dqkv) * GROUP_WIDTH + sum(t.shape[1] for t in tails)

    def body(*refs):
        o_ref = refs[-1]
        col = 0
        for ref in refs[:len(dqkv)]:
            for j in range(HEADS_PER_GROUP):
                o_ref[:, col:col + HEAD_DIM] = ref[j].astype(BF16)
                col += HEAD_DIM
        for ref in refs[len(dqkv):-1]:
            o_ref[:, col:col + ref.shape[1]] = ref[...]
            col += ref.shape[1]

    heads = pl.BlockSpec((HEADS_PER_GROUP, ROW_TILE, HEAD_DIM), lambda i: (0, i, 0))
    return pl.pallas_call(
        body, name="dproj_assemble", grid=(S // ROW_TILE,),
        in_specs=[heads] * len(dqkv) + [_row_spec(t.shape[1]) for t in tails],
        out_specs=_row_spec(width), out_shape=jax.ShapeDtypeStruct((S, width), BF16),
        compiler_params=_params(("parallel",)),
    )(*dqkv, *tails)


def _tap_rows(xpad_ref, t0, k, width, pad):
    return xpad_ref[pl.ds(t0 + (pad - (width - 1 - k)), TIME_BLOCK), :]


def _conv_block(xpad_ref, t0, w_ref, width, pad):
    acc = None
    for k in range(width):
        term = w_ref[k:k + 1, :] * _tap_rows(xpad_ref, t0, k, width, pad)
        acc = term if acc is None else acc + term
    return acc


def _conv_transpose_block(dpad_ref, t0, w_ref, width):
    acc = None
    for k in range(width):
        term = w_ref[k:k + 1, :] * dpad_ref[pl.ds(t0 + (width - 1 - k), TIME_BLOCK), :]
        acc = term if acc is None else acc + term
    return acc


def _conv_weight_grad(xpad_ref, t0, dy, dw_ref, width, pad):
    for k in range(width):
        dw_ref[k:k + 1, :] += jnp.sum(dy * _tap_rows(xpad_ref, t0, k, width, pad), axis=0, keepdims=True)


def _time_loop(S, step, skip_first=0, skip_last=0):
    def it(tb, carry):
        step(pl.multiple_of(tb * TIME_BLOCK, TIME_BLOCK))
        return carry

    lax.fori_loop(skip_first, S // TIME_BLOCK - skip_last, it, 0)


def _fill_head(head_ref, x_ref, pad):
    head_ref[0:pad, :] = jnp.zeros((pad, LANES), F32)
    head_ref[pad:, :] = x_ref[0:TIME_BLOCK, :]


def _fill_tail(tail_ref, x_ref, pad):
    S = x_ref.shape[0]
    tail_ref[0:TIME_BLOCK, :] = x_ref[S - TIME_BLOCK:S, :]
    tail_ref[TIME_BLOCK:, :] = jnp.zeros((pad, LANES), F32)


def _conv_fwd_call(proj, col0, w, b):
    S = proj.shape[0]
    C = w.shape[1]
    nt = C // LANES
    v0, g0 = col0 // LANES, (col0 + C) // LANES

    def body(val_ref, gate_ref, w_ref, b_ref, o_ref, pad_ref):
        pad_ref[0:CONV_PAD, :] = jnp.zeros((CONV_PAD, LANES), F32)
        pad_ref[CONV_PAD:, :] = val_ref[...] * _sigmoid(gate_ref[...])

        def step(t0):
            o_ref[pl.ds(t0, TIME_BLOCK), :] = _conv_block(pad_ref, t0, w_ref, CONV_WIDTH, CONV_PAD) + b_ref[...]

        _time_loop(S, step)

    seq = lambda off: pl.BlockSpec((S, LANES), lambda i: (0, off + i))
    return pl.pallas_call(
        body, name="conv_module", grid=(nt,),
        in_specs=[seq(v0), seq(g0), pl.BlockSpec((CONV_WIDTH, LANES), lambda i: (0, i)), pl.BlockSpec((1, LANES), lambda i: (0, i))],
        out_specs=seq(0), out_shape=jax.ShapeDtypeStruct((S, C), F32),
        scratch_shapes=[pltpu.VMEM((S + CONV_PAD, LANES), F32)],
        compiler_params=_params(("parallel",)),
    )(proj, proj, w, b)


def _conv_bwd_call(proj, col0, w, dc1):
    S = proj.shape[0]
    C = w.shape[1]
    nt = C // LANES
    v0, g0 = col0 // LANES, (col0 + C) // LANES

    def body(val_ref, gate_ref, w_ref, dy_ref, dval_ref, dgate_ref, dw_ref, db_ref, xpad_ref, tail_ref, dwacc_ref):
        xpad_ref[0:CONV_PAD, :] = jnp.zeros((CONV_PAD, LANES), F32)
        xpad_ref[CONV_PAD:, :] = val_ref[...] * _sigmoid(gate_ref[...])
        _fill_tail(tail_ref, dy_ref, CONV_PAD)
        dwacc_ref[...] = jnp.zeros_like(dwacc_ref)

        def block(t0, dy_src, dy_t0):
            rows = pl.ds(t0, TIME_BLOCK)
            _conv_weight_grad(xpad_ref, t0, dy_ref[rows, :], dwacc_ref, CONV_WIDTH, CONV_PAD)
            dc0 = _conv_transpose_block(dy_src, dy_t0, w_ref, CONV_WIDTH)
            sg = _sigmoid(gate_ref[rows, :])
            dval_ref[rows, :] = (dc0 * sg).astype(BF16)
            dgate_ref[rows, :] = (dc0 * val_ref[rows, :] * sg * (1.0 - sg)).astype(BF16)

        _time_loop(S, lambda t0: block(t0, dy_ref, t0), skip_last=1)
        block(S - TIME_BLOCK, tail_ref, 0)
        dw_ref[...] = dwacc_ref[...]
        db_ref[...] = jnp.sum(dy_ref[...], axis=0, keepdims=True)

    seq = lambda off: pl.BlockSpec((S, LANES), lambda i: (0, off + i))
    return pl.pallas_call(
        body, name="conv_module_bwd", grid=(nt,),
        in_specs=[seq(v0), seq(g0), pl.BlockSpec((CONV_WIDTH, LANES), lambda i: (0, i)), seq(0)],
        out_specs=[seq(0), seq(0), pl.BlockSpec((CONV_PAD, LANES), lambda i: (0, i)), pl.BlockSpec((1, LANES), lambda i: (0, i))],
        out_shape=[jax.ShapeDtypeStruct((S, C), BF16), jax.ShapeDtypeStruct((S, C), BF16),
                   jax.ShapeDtypeStruct((CONV_PAD, C), F32), jax.ShapeDtypeStruct((1, C), F32)],
        scratch_shapes=[pltpu.VMEM((S + CONV_PAD, LANES), F32), pltpu.VMEM((TIME_BLOCK + CONV_PAD, LANES), F32),
                        pltpu.VMEM((CONV_PAD, LANES), F32)],
        compiler_params=_params(("parallel",)),
    )(proj, proj, w, dc1)


def _ffn_fwd_call(u, w, b):
    S, C2 = u.shape
    C = C2 // 2
    nt = C // LANES

    def body(ug_ref, uv_ref, wg_ref, wv_ref, bg_ref, bv_ref, f_ref, hg_ref, hv_ref):
        _fill_head(hg_ref, ug_ref, FFN_PAD)
        _fill_head(hv_ref, uv_ref, FFN_PAD)

        def block(t0, xg_ref, xv_ref, x_t0, pad):
            cg = _conv_block(xg_ref, x_t0, wg_ref, FFN_CONV_WIDTH, pad) + bg_ref[...]
            cv = _conv_block(xv_ref, x_t0, wv_ref, FFN_CONV_WIDTH, pad) + bv_ref[...]
            f_ref[pl.ds(t0, TIME_BLOCK), :] = (_gelu(cg) * cv).astype(BF16)

        block(0, hg_ref, hv_ref, 0, FFN_PAD)
        _time_loop(S, lambda t0: block(t0, ug_ref, uv_ref, t0, 0), skip_first=1)

    seq = lambda off: pl.BlockSpec((S, LANES), lambda i: (0, off + i))
    wsp = lambda off: pl.BlockSpec((FFN_CONV_WIDTH, LANES), lambda i: (0, off + i))
    bsp = lambda off: pl.BlockSpec((1, LANES), lambda i: (0, off + i))
    return pl.pallas_call(
        body, name="ffn_conv_geglu", grid=(nt,),
        in_specs=[seq(0), seq(nt), wsp(0), wsp(nt), bsp(0), bsp(nt)],
        out_specs=seq(0), out_shape=jax.ShapeDtypeStruct((S, C), BF16),
        scratch_shapes=[pltpu.VMEM((FFN_PAD + TIME_BLOCK, LANES), F32)] * 2,
        compiler_params=_params(("parallel",)),
    )(u, u, w, w, b, b)


def _ffn_bwd_call(u, w, b, df):
    S, C2 = u.shape
    C = C2 // 2
    nt = C // LANES

    def body(ug_ref, uv_ref, wg_ref, wv_ref, bg_ref, bv_ref, df_ref,
             du_ref, dwg_ref, dwv_ref, dbg_ref, dbv_ref,
             hg_ref, hv_ref, dg_ref, dv_ref, dwg_acc, dwv_acc, dbg_acc, dbv_acc):
        zeros = jnp.zeros((FFN_PAD, LANES), F32)
        _fill_head(hg_ref, ug_ref, FFN_PAD)
        _fill_head(hv_ref, uv_ref, FFN_PAD)
        dg_ref[S:, :] = zeros
        dv_ref[S:, :] = zeros
        dwg_acc[...] = jnp.zeros_like(dwg_acc)
        dwv_acc[...] = jnp.zeros_like(dwv_acc)
        dbg_acc[...] = jnp.zeros_like(dbg_acc)
        dbv_acc[...] = jnp.zeros_like(dbv_acc)

        def first(t0, xg_ref, xv_ref, x_t0, pad):
            rows = pl.ds(t0, TIME_BLOCK)
            cg = _conv_block(xg_ref, x_t0, wg_ref, FFN_CONV_WIDTH, pad) + bg_ref[...]
            cv = _conv_block(xv_ref, x_t0, wv_ref, FFN_CONV_WIDTH, pad) + bv_ref[...]
            dfb = df_ref[rows, :]
            gelu, gelu_grad = _gelu_and_grad(cg)
            dcg = dfb * cv * gelu_grad
            dcv = dfb * gelu
            dg_ref[rows, :] = dcg
            dv_ref[rows, :] = dcv
            _conv_weight_grad(xg_ref, x_t0, dcg, dwg_acc, FFN_CONV_WIDTH, pad)
            _conv_weight_grad(xv_ref, x_t0, dcv, dwv_acc, FFN_CONV_WIDTH, pad)
            dbg_acc[...] += jnp.sum(dcg, axis=0, keepdims=True)
            dbv_acc[...] += jnp.sum(dcv, axis=0, keepdims=True)

        def second(t0):
            rows = pl.ds(t0, TIME_BLOCK)
            du_ref[0, rows, :] = _conv_transpose_block(dg_ref, t0, wg_ref, FFN_CONV_WIDTH).astype(BF16)
            du_ref[1, rows, :] = _conv_transpose_block(dv_ref, t0, wv_ref, FFN_CONV_WIDTH).astype(BF16)

        first(0, hg_ref, hv_ref, 0, FFN_PAD)
        _time_loop(S, lambda t0: first(t0, ug_ref, uv_ref, t0, 0), skip_first=1)
        _time_loop(S, second)
        dwg_ref[...] = dwg_acc[...]
        dwv_ref[...] = dwv_acc[...]
        dbg_ref[...] = dbg_acc[...]
        dbv_ref[...] = dbv_acc[...]

    seq = lambda off: pl.BlockSpec((S, LANES), lambda i: (0, off + i))
    wsp = lambda off: pl.BlockSpec((FFN_CONV_WIDTH, LANES), lambda i: (0, off + i))
    bsp = lambda off: pl.BlockSpec((1, LANES), lambda i: (0, off + i))
    return pl.pallas_call(
        body, name="ffn_conv_geglu_bwd", grid=(nt,),
        in_specs=[seq(0), seq(nt), wsp(0), wsp(nt), bsp(0), bsp(nt), seq(0)],
        out_specs=[pl.BlockSpec((2, S, LANES), lambda i: (0, 0, i)),
                   pl.BlockSpec((SUBLANES, LANES), lambda i: (0, i)), pl.BlockSpec((SUBLANES, LANES), lambda i: (0, i)),
                   bsp(0), bsp(0)],
        out_shape=[jax.ShapeDtypeStruct((2, S, C), BF16)] + [jax.ShapeDtypeStruct((SUBLANES, C), F32)] * 2
        + [jax.ShapeDtypeStruct((1, C), F32)] * 2,
        scratch_shapes=[pltpu.VMEM((FFN_PAD + TIME_BLOCK, LANES), F32)] * 2 + [pltpu.VMEM((S + FFN_PAD, LANES), F32)] * 2
        + [pltpu.VMEM((SUBLANES, LANES), F32)] * 2
        + [pltpu.VMEM((1, LANES), F32)] * 2,
        compiler_params=_params(("parallel",)),
    )(u, u, w, w, b, b, df)


def _adamw(w_ref, g_ref, m_ref, v_ref, d_ref, mo_ref, vo_ref):
    gv = g_ref[...]
    mn = ADAM_B1 * m_ref[...] + (1.0 - ADAM_B1) * gv
    vn = ADAM_B2 * v_ref[...] + (1.0 - ADAM_B2) * (gv * gv)
    mo_ref[...] = mn
    vo_ref[...] = vn
    m_hat = mn * (1.0 / (1.0 - ADAM_B1 ** ADAM_STEP))
    v_hat = vn * (1.0 / (1.0 - ADAM_B2 ** ADAM_STEP))
    d_ref[...] = -ADAM_LR * (m_hat / (jnp.sqrt(v_hat) + ADAM_EPS) + ADAM_WD * w_ref[...])


def _adamw_call(w, g, m, v, name):
    R, C = w.shape
    tr = _row_tile(R, C)

    def body(w_ref, g_ref, m_ref, v_ref, go_ref, d_ref, mo_ref, vo_ref):
        go_ref[...] = g_ref[...]
        _adamw(w_ref, g_ref, m_ref, v_ref, d_ref, mo_ref, vo_ref)

    spec = pl.BlockSpec((tr, C), lambda i: (i, 0))
    return pl.pallas_call(
        body, name=name, grid=(R // tr,),
        in_specs=[spec] * 4, out_specs=[spec] * 4,
        out_shape=[jax.ShapeDtypeStruct((R, C), F32)] * 4,
        compiler_params=_params(("parallel",)),
    )(w, g, m, v)


def _adamw_small_call(ws, gs, ms, vs):
    n = len(ws)

    def body(*refs):
        w_refs, g_refs, m_refs, v_refs, d_refs, mo_refs, vo_refs = (refs[i * n:(i + 1) * n] for i in range(7))
        for i in range(n):
            _adamw(w_refs[i], g_refs[i], m_refs[i], v_refs[i], d_refs[i], mo_refs[i], vo_refs[i])

    whole = pl.BlockSpec(memory_space=pltpu.VMEM)
    outs = pl.pallas_call(
        body, name="adamw_small",
        in_specs=[whole] * (4 * n), out_specs=[whole] * (3 * n),
        out_shape=[jax.ShapeDtypeStruct(w.shape, F32) for w in ws] * 3,
    )(*ws, *gs, *ms, *vs)
    return outs[:n], outs[n:2 * n], outs[2 * n:]


def _position():
    return lax.axis_index("x"), lax.axis_index("y"), lax.axis_index("c")


def _chip_peers(x, y):
    return [(x, 1 - y), (1 - x, y), (1 - x, 1 - y)]


def _half_rows(ref, core, rows):
    h = rows // 2
    start = pl.multiple_of(core * h, PACKED_ROWS)
    return ref.at[pl.ds(start, h), :] if len(ref.shape) == 2 else ref.at[:, pl.ds(start, h), :]


def _shard_half(ref, shard, core, rows):
    h = rows // 2
    return ref.at[shard, pl.ds(pl.multiple_of(core * h, PACKED_ROWS), h), :]


ANY = pl.BlockSpec(memory_space=pl.ANY)


def _first_hop_copies(srcs, lands):
    x, y, c = _position()
    chip = 2 * x + y
    targets = [(px, py, c) for px, py in _chip_peers(x, y)] + [(x, y, 1 - c)]
    rows = srcs[0].shape[0]
    out = []
    for i, (s, l) in enumerate(zip(srcs, lands)):
        for k, dev in enumerate(targets):
            if i == 0 and k < 3:
                out.append((_half_rows(s, c, rows), _shard_half(l, chip, c, rows), dev, k))
            else:
                out.append((s, l.at[chip], dev, len(targets) * i + k))
    return out


def _second_hop_copies(srcs, lands):
    x, y, c = _position()
    rows = lands[0].shape[1]
    out = []
    for k, (px, py) in enumerate(_chip_peers(x, y)):
        half = _shard_half(lands[0], 2 * px + py, c, rows)
        out.append((half, half, (x, y, 1 - c), k))
    return out


HBM_SPEC = pl.BlockSpec(memory_space=pltpu.HBM)
SEM_SPEC = pl.BlockSpec(memory_space=pltpu.SEMAPHORE)
DATAFLOW = pltpu.SideEffectType.DATAFLOW_SIDE_EFFECTING


def _in_hbm(a):
    return pltpu.with_memory_space_constraint(a, pltpu.HBM)


def _split_start(name, groups, after, carry=None):
    spans, arrays = [], []
    for srcs, lands, _, _ in groups:
        spans.append((len(arrays), len(srcs), len(lands)))
        arrays += list(srcs) + list(lands)
    if carry is not None:
        arrays.append(carry)
    na, ng = len(arrays), len(groups)

    def body(*refs):
        sems, token = refs[na + 1:na + 1 + 2 * ng], refs[-1]
        for g, (_, _, _, copies) in enumerate(groups):
            off, ns, nl = spans[g]
            for src, dst, dev, idx in copies(refs[off:off + ns], refs[off + ns:off + ns + nl]):
                pltpu.make_async_remote_copy(src_ref=src, dst_ref=dst, send_sem=sems[2 * g].at[idx], recv_sem=sems[2 * g + 1].at[idx],
                                             device_id=dev, device_id_type=MESH).start()
        token[...] = jnp.zeros_like(token)

    outs = pl.pallas_call(
        body, name=name,
        in_specs=[HBM_SPEC] * na + [ANY],
        out_specs=[SEM_SPEC] * (2 * ng) + [HBM_SPEC] * na + [pl.BlockSpec(memory_space=pltpu.VMEM)],
        out_shape=[pltpu.SemaphoreType.DMA((n_sems,)) for _, _, n_sems, _ in groups for _ in range(2)]
        + [pltpu.HBM(a.shape, a.dtype) for a in arrays] + [jax.ShapeDtypeStruct((SUBLANES, LANES), F32)],
        input_output_aliases={i: 2 * ng + i for i in range(na)},
        compiler_params=pltpu.CompilerParams(has_side_effects=DATAFLOW),
    )(*[_in_hbm(a) for a in arrays], after)
    started = []
    for g, (off, ns, nl) in enumerate(spans):
        thru = outs[2 * ng + off:2 * ng + off + ns + nl]
        started.append(dict(send=outs[2 * g], recv=outs[2 * g + 1], srcs=list(thru[:ns]), lands=list(thru[ns:]),
                            tile=outs[-1], token=outs[-1][0, 0], carry=None if carry is None else outs[2 * ng + na - 1]))
    return started


def _split_wait(name, started, copies, after):
    n, m = len(started["srcs"]), len(started["lands"])

    def body(*refs):
        src_refs, land_refs = refs[:n], refs[n:n + m]
        send_sem, recv_sem = refs[n + m], refs[n + m + 1]
        for src, dst, dev, idx in copies(src_refs, land_refs):
            cp = pltpu.make_async_remote_copy(src_ref=src, dst_ref=dst, send_sem=send_sem.at[idx], recv_sem=recv_sem.at[idx],
                                              device_id=dev, device_id_type=MESH)
            cp.wait_send()
            cp.wait_recv()

    arrays = started["srcs"] + started["lands"]
    outs = pl.pallas_call(
        body, name=name,
        in_specs=[HBM_SPEC] * (n + m) + [SEM_SPEC, SEM_SPEC, ANY],
        out_specs=[HBM_SPEC] * (n + m),
        out_shape=[pltpu.HBM(a.shape, a.dtype) for a in arrays],
        input_output_aliases={i: i for i in range(n + m)},
        compiler_params=pltpu.CompilerParams(has_side_effects=DATAFLOW),
    )(*arrays, started["send"], started["recv"], after)
    return list(outs)


def _gather_copies(srcs, lands):
    x, y, c = _position()
    chip = 2 * x + y
    targets = [(px, py, c) for px, py in _chip_peers(x, y)] + [(x, y, 1 - c)]
    return [(s, l.at[chip], dev, len(targets) * i + k) for i, (s, l) in enumerate(zip(srcs, lands)) for k, dev in enumerate(targets)]


def _sibling_copies(srcs, lands):
    x, y, c = _position()
    return [(_half_rows(srcs[0], 1 - c, srcs[0].shape[1]), lands[0], (x, y, 1 - c), 0)]


def _sibling_whole_copies(srcs, lands):
    x, y, c = _position()
    return [(srcs[0], lands[0], (x, y, 1 - c), 0)]


def _exchange_copies(srcs, lands):
    x, y, c = _position()
    return [(srcs[0].at[2 * px + py], lands[0].at[k], (px, py, c), k) for k, (px, py) in enumerate(_chip_peers(x, y))]


def _pair_sum_call(grad, recv, chip_core, name):
    _, h, B = recv.shape
    tr = _row_tile(h, B)

    def body(cc_ref, g_ref, r_ref, o_ref, ob_ref):
        s = g_ref[...] + r_ref[...]
        ob_ref[...] = s.astype(BF16)

        @pl.when(pl.program_id(1) == cc_ref[0])
        def _():
            o_ref[...] = s

    g_spec = pl.BlockSpec((None, tr, B), lambda i, q, cc_ref: (q, cc_ref[1] * (h // tr) + i, 0))
    spec = pl.BlockSpec((None, tr, B), lambda i, q, cc_ref: (q, i, 0))
    own_spec = pl.BlockSpec((tr, B), lambda i, q, cc_ref: (i, 0))
    return pl.pallas_call(
        body, name=name,
        grid_spec=pltpu.PrefetchScalarGridSpec(num_scalar_prefetch=1, grid=(h // tr, N_CHIPS), in_specs=[g_spec, spec],
                                               out_specs=[own_spec, spec]),
        out_shape=[jax.ShapeDtypeStruct((h, B), F32), jax.ShapeDtypeStruct(recv.shape, BF16)],
        compiler_params=_params(("parallel", "arbitrary")),
    )(chip_core, grad, recv)


def _chip_sum_call(partial, recv, chip_core, name):
    _, h, B = recv.shape
    tr = _row_tile(h, B)

    def body(cc_ref, p_ref, r_ref, o_ref):
        o_ref[...] = ((p_ref[...] + r_ref[0].astype(F32)) + r_ref[1].astype(F32)) + r_ref[2].astype(F32)

    return pl.pallas_call(
        body, name=name,
        grid_spec=pltpu.PrefetchScalarGridSpec(
            num_scalar_prefetch=1, grid=(h // tr,),
            in_specs=[pl.BlockSpec((tr, B), lambda i, cc_ref: (i, 0)),
                      pl.BlockSpec((3, tr, B), lambda i, cc_ref: (0, i, 0))],
            out_specs=pl.BlockSpec((tr, B), lambda i, cc_ref: (cc_ref[1] * (h // tr) + i, 0))),
        out_shape=jax.ShapeDtypeStruct((2 * h, B), F32),
        compiler_params=_params(("parallel",)),
    )(chip_core, partial, recv)


def _sibling_assemble_call(shards, name="grad_sibling_assemble"):
    n = len(shards)

    def body(*refs):
        ins, outs = refs[:n], refs[n:2 * n]
        send_sems, recv_sems = refs[2 * n:]
        x, y, c = _position()
        copies = []
        for i in range(n):
            rows = shards[i].shape[0]
            cp = pltpu.make_async_remote_copy(src_ref=_half_rows(ins[i], c, rows), dst_ref=_half_rows(outs[i], c, rows),
                                              send_sem=send_sems.at[i], recv_sem=recv_sems.at[i],
                                              device_id=(x, y, 1 - c), device_id_type=MESH)
            cp.start()
            copies.append(cp)
        for cp in copies:
            cp.wait()

    return pl.pallas_call(
        body, name=name,
        in_specs=[ANY] * n, out_specs=[ANY] * n,
        out_shape=[jax.ShapeDtypeStruct(s.shape, F32) for s in shards],
        input_output_aliases={i: i for i in range(n)},
        scratch_shapes=[pltpu.SemaphoreType.DMA((n,)), pltpu.SemaphoreType.DMA((n,))],
    )(*shards)


N_DEVICES = 8


def _allsum_copies(srcs, lands):
    x, y, c = _position()
    me = 4 * x + 2 * y + c
    out = []
    for k in range(1, N_DEVICES):
        peer = (1 - x if k & 4 else x, 1 - y if k & 2 else y, 1 - c if k & 1 else c)
        out.append((srcs[0], lands[0].at[me], peer, k - 1))
    return out


def _ordered_sum_call(mine, landed, me_chip, shapes, sharded_cols):
    rows = mine.shape[0]
    outs = [(s[0], n) if n else s for s, n in zip(shapes, sharded_cols)]

    def body(mc_ref, x_ref, l_ref, *refs):
        acc_ref = refs[-1]
        acc = jnp.where(mc_ref[0] == 0, x_ref[...], l_ref[0])
        for d in range(1, N_DEVICES):
            acc = acc + jnp.where(mc_ref[0] == d, x_ref[...], l_ref[d])
        acc_ref[...] = acc
        first = 0
        for o_ref, (r, c), n in zip(refs[:-1], shapes, sharded_cols):
            per_row = c // LANES

            def unpack(chip, o_ref=o_ref, r=r, n=n, per_row=per_row, first=first):
                for i in range(r):
                    for j in range((n or per_row * LANES) // LANES):
                        src = first + i * per_row + chip * ((n or 0) // LANES) + j
                        o_ref[i:i + 1, j * LANES:(j + 1) * LANES] = acc_ref[src:src + 1, :]

            if n:
                for q in range(N_CHIPS):
                    pl.when(mc_ref[1] == q)(functools.partial(unpack, q))
            else:
                unpack(0)
            first += r * per_row

    results = pl.pallas_call(
        body, name="small_grad_sum",
        in_specs=[pl.BlockSpec(memory_space=pltpu.SMEM), pl.BlockSpec(memory_space=pltpu.VMEM), pl.BlockSpec(memory_space=pltpu.VMEM)],
        out_specs=[pl.BlockSpec(memory_space=pltpu.VMEM)] * len(outs),
        out_shape=[jax.ShapeDtypeStruct(s, F32) for s in outs],
        scratch_shapes=[pltpu.VMEM((rows, LANES), F32)],
    )(me_chip, mine, landed)
    return results


def _pack(arrays):
    flat = jnp.concatenate([a.reshape(-1).astype(F32) for a in arrays])
    rows = -(-flat.shape[0] // LANES)
    rows = -(-rows // SUBLANES) * SUBLANES
    flat = jnp.pad(flat, (0, rows * LANES - flat.shape[0]))
    return flat.reshape(rows, LANES)


def _local_step(xs, target, P, late_weights, on_grad):
    S, D = xs.shape
    qkv_width = 3 * N_HEADS * HEAD_DIM
    glu_col0, gate_col0 = qkv_width, qkv_width + 2 * D
    shard_major = lambda g: g.reshape(N_CHIPS, g.shape[0] // N_CHIPS, g.shape[1])

    h1 = _rms_fwd_call(xs, P["norm_mix_pre"])
    buckets = _bucket_tables()
    bias = _bias_table_call(P["rel_bias"] + 0.0 * h1[0, 0].astype(F32), buckets)
    P = dict(P, **late_weights("in", bias))
    proj = _matmul(h1, P["w_in"], "nn", "proj_in")
    parts = []
    for g in range(N_GROUPS):
        parts += _attn_fwd_call(proj, bias, g)
    a, a_bf, lse = _attn_merge_call(parts)
    P = dict(P, **late_weights("mix", a_bf))
    y_a = _matmul(a_bf, P["w_attn_out"], "nn", "attn_out")
    c1 = _conv_fwd_call(proj, glu_col0, P["conv_dw_w"], P["conv_dw_b"])
    cact = _ln_silu_call(c1, P["conv_ln_g"], P["conv_ln_b"])
    y_c = _matmul(cact, P["conv_pw_w"], "nn", "conv_pw")
    mixed = _mix_call(proj, gate_col0, P["b_gate"], y_a, y_c)
    out = _matmul(mixed, P["w_out"], "nn", "mix_out")
    x1, h2 = _res1_call(xs, out, P["norm_mix_post"], P["norm_ffn_pre"])
    P = dict(P, **late_weights("up", h2))
    u = _matmul(h2, P["w_up"], "nn", "ffn_up")
    f = _ffn_fwd_call(u, P["ffn_conv_w"], P["ffn_conv_b"])
    P = dict(P, **late_weights("down", f))
    yff = _matmul(f, P["w_down"], "nn", "ffn_down")
    loss_tile, dx2, dyff, dg_ffn_post = _loss_call(yff, x1, P["norm_ffn_post"], target)

    G = {}
    G["norm_ffn_post"] = dg_ffn_post
    on_grad("w_down", shard_major(_matmul(f, dyff, "tn", "ffn_down_dw")))
    df = _matmul(dyff, P["w_down"], "nt", "ffn_down_dx")
    du, dwg, dwv, dbg, dbv = _ffn_bwd_call(u, P["ffn_conv_w"], P["ffn_conv_b"], df)
    G["ffn_conv_w"] = jnp.concatenate([dwg[:FFN_CONV_WIDTH], dwv[:FFN_CONV_WIDTH]], axis=1)
    G["ffn_conv_b"] = jnp.concatenate([dbg, dbv], axis=1)
    du = on_grad("w_up", functools.partial(_grad_half_matmul, h2, "ffn_up_dw"), carry=du)
    dh2 = _matmul(du, P["w_up"], "nt", "ffn_up_dx")
    dx1, dout, G["norm_ffn_pre"], G["norm_mix_post"] = _mid_bwd_call(x1, P["norm_ffn_pre"], dh2, dx2, out, P["norm_mix_post"])
    on_grad("w_out", shard_major(_matmul(mixed, dout, "tn", "mix_out_dw")))
    dmixed = _matmul(dout, P["w_out"], "nt", "mix_out_dx")
    dya, dyc, dga, dgc, dba, dbc = _mix_bwd_call(dmixed, proj, gate_col0, P["b_gate"], y_a, y_c)
    G["b_gate"] = jnp.concatenate([dba, dbc], axis=1)
    on_grad("w_attn_out", _matmul(a_bf, dya, "tn", "attn_out_dw", out_shards=True))
    dyc = on_grad("conv_pw_w", shard_major(_matmul(cact, dyc, "tn", "conv_pw_dw")), carry=dyc)
    da = _matmul(dya, P["w_attn_out"], "nt", "attn_out_dx")
    dcact = _matmul(dyc, P["conv_pw_w"], "nt", "conv_pw_dx")
    dc1, G["conv_ln_g"], G["conv_ln_b"] = _ln_silu_bwd_call(c1, P["conv_ln_g"], P["conv_ln_b"], dcact)
    dval, dgate, dw_dw, G["conv_dw_b"] = _conv_bwd_call(proj, glu_col0, P["conv_dw_w"], dc1)
    G["conv_dw_w"] = dw_dw[:CONV_WIDTH]
    delta = _attn_delta_call(a, da)
    dqs, dks, dvs, dbs = [], [], [], []
    for g in range(N_GROUPS):
        dq, dk, dv, db = _attn_bwd_call(proj, bias, da, lse, delta, g)
        dqs.append(dq)
        dks.append(dk)
        dvs.append(dv)
        dbs.append(db)
    G["rel_bias"] = _bias_grad_call(jnp.concatenate(dbs, axis=0), buckets)
    dproj = _dproj_call(dqs + dks + dvs, [dval, dgate, dga, dgc])
    dproj = on_grad("w_in", functools.partial(_grad_half_matmul, h1, "proj_in_dw"), carry=dproj)
    dh1 = _matmul(dproj, P["w_in"], "nt", "proj_in_dx")
    dh1 = on_grad(None, None, carry=dh1)
    grad_x, G["norm_mix_pre"] = _in_bwd_call(xs, P["norm_mix_pre"], dh1, dx1)
    return loss_tile, grad_x, G


def kernel(x, w_in, b_gate, rel_bias, w_attn_out, conv_dw_w, conv_dw_b, conv_ln_g, conv_ln_b, conv_pw_w, w_out, norm_mix_pre, norm_mix_post, norm_ffn_pre, norm_ffn_post, w_up, ffn_conv_w, ffn_conv_b, w_down, loss_target, m_w_in, m_b_gate, m_rel_bias, m_w_attn_out, m_conv_dw_w, m_conv_dw_b, m_conv_ln_g, m_conv_ln_b, m_conv_pw_w, m_w_out, m_norm_mix_pre, m_norm_mix_post, m_norm_ffn_pre, m_norm_ffn_post, m_w_up, m_ffn_conv_w, m_ffn_conv_b, m_w_down, v_w_in, v_b_gate, v_rel_bias, v_w_attn_out, v_conv_dw_w, v_conv_dw_b, v_conv_ln_g, v_conv_ln_b, v_conv_pw_w, v_w_out, v_norm_mix_pre, v_norm_mix_post, v_norm_ffn_pre, v_norm_ffn_post, v_w_up, v_ffn_conv_w, v_ffn_conv_b, v_w_down):
    weights = dict(w_in=w_in, b_gate=b_gate, rel_bias=rel_bias, w_attn_out=w_attn_out, conv_dw_w=conv_dw_w, conv_dw_b=conv_dw_b,
                   conv_ln_g=conv_ln_g, conv_ln_b=conv_ln_b, conv_pw_w=conv_pw_w, w_out=w_out, norm_mix_pre=norm_mix_pre,
                   norm_mix_post=norm_mix_post, norm_ffn_pre=norm_ffn_pre, norm_ffn_post=norm_ffn_post, w_up=w_up,
                   ffn_conv_w=ffn_conv_w, ffn_conv_b=ffn_conv_b, w_down=w_down)
    m_in = dict(w_in=m_w_in, b_gate=m_b_gate, rel_bias=m_rel_bias, w_attn_out=m_w_attn_out, conv_dw_w=m_conv_dw_w,
                conv_dw_b=m_conv_dw_b, conv_ln_g=m_conv_ln_g, conv_ln_b=m_conv_ln_b, conv_pw_w=m_conv_pw_w, w_out=m_w_out,
                norm_mix_pre=m_norm_mix_pre, norm_mix_post=m_norm_mix_post, norm_ffn_pre=m_norm_ffn_pre,
                norm_ffn_post=m_norm_ffn_post, w_up=m_w_up, ffn_conv_w=m_ffn_conv_w, ffn_conv_b=m_ffn_conv_b, w_down=m_w_down)
    v_in = dict(w_in=v_w_in, b_gate=v_b_gate, rel_bias=v_rel_bias, w_attn_out=v_w_attn_out, conv_dw_w=v_conv_dw_w,
                conv_dw_b=v_conv_dw_b, conv_ln_g=v_conv_ln_g, conv_ln_b=v_conv_ln_b, conv_pw_w=v_conv_pw_w, w_out=v_w_out,
                norm_mix_pre=v_norm_mix_pre, norm_mix_post=v_norm_mix_post, norm_ffn_pre=v_norm_ffn_pre,
                norm_ffn_post=v_norm_ffn_post, w_up=v_w_up, ffn_conv_w=v_ffn_conv_w, ffn_conv_b=v_ffn_conv_b, w_down=v_w_down)
    names = list(weights)
    xi, yi, ci = _position()
    chip = 2 * xi + yi
    core_arr = jnp.reshape(ci, (1,)).astype(jnp.int32)

    xs = x[0]
    target = loss_target[0]
    S, D = xs.shape

    big = ["w_in", "w_attn_out", "conv_pw_w", "w_out", "w_up", "w_down"]
    row_sharded = ("conv_pw_w", "w_out", "w_down")
    natural = lambda k, g: g.reshape(-1, g.shape[2]) if k in row_sharded else g
    first_srcs = [w_in[0].astype(BF16), conv_dw_w[0], ffn_conv_w[0]]
    first_lands = [lax.empty((N_CHIPS,) + s.shape, s.dtype) for s in first_srcs]
    (first_hop,) = _split_start("gather_in_start", [(first_srcs, first_lands, 4 * len(first_srcs), _first_hop_copies)], core_arr)
    launched = first_hop["token"]
    late_sets = dict(mix=["w_attn_out", "conv_pw_w", "w_out"], up=["w_up"], down=["w_down"])
    late_groups = []
    for keys in late_sets.values():
        srcs = [(weights[k][0] + launched).astype(BF16) for k in keys]
        late_groups.append((srcs, [lax.empty((N_CHIPS,) + s.shape, BF16) for s in srcs], 4 * len(keys), _gather_copies))
    started = {}

    def late_weights(tag, after):
        if tag == "in":
            w_in_halves, dw4, fc4 = _split_wait("gather_in_wait", first_hop, _first_hop_copies, after)[len(first_srcs):]
            second_hop, *late = _split_start("gather_in_pass_start", [([], [w_in_halves], 3, _second_hop_copies)] + late_groups, dw4)
            started.update(zip(late_sets, late))
            (w_in_full,) = _split_wait("gather_in_pass_wait", second_hop, _second_hop_copies, second_hop["tile"])
            return dict(w_in=w_in_full, conv_dw_w=jnp.concatenate(list(dw4), axis=1), ffn_conv_w=jnp.concatenate(list(fc4), axis=1))
        landed = _split_wait(f"gather_{tag}_wait", started[tag], _gather_copies, after)[len(late_sets[tag]):]
        return {k: natural(k, g) for k, g in zip(late_sets[tag], landed)}

    chip_core = jnp.stack([chip, ci]).astype(jnp.int32)
    exchanging, pending, second_half = {}, {}, {}

    held = []

    def launch(tag, after, carry=None):
        keys, groups, partial = [], [], {}
        for k in list(exchanging):
            st, copies = exchanging.pop(k)
            gk, r1 = _split_wait(f"sibling_exchange_wait_{k}", st, copies, after)
            if k in second_half:
                partial[k], s16 = second_half.pop(k)(init=r1)
            else:
                partial[k], s16 = _pair_sum_call(gk, r1, chip_core, f"pair_sum_{k}")
            keys.append(k)
            groups.append(([s16], [lax.empty((3,) + s16.shape[1:], BF16)], 3, _exchange_copies))
        fresh = [(k, copies) for k, _, copies in held]
        for _, g3, copies in held:
            rows = g3.shape[1] // 2 if copies is _sibling_copies else g3.shape[1]
            groups.append(([g3], [lax.empty((N_CHIPS, rows, g3.shape[2]), F32)], 1, copies))
        held.clear()
        begun = _split_start(f"grad_exchange_start_{tag}", groups, core_arr, carry)
        for k, st in zip(keys, begun):
            pending[k] = (partial[k], st)
        for (k, copies), st in zip(fresh, begun[len(keys):]):
            exchanging[k] = (st, copies)
        return begun[0]["carry"]

    def on_grad(k, g, carry=None):
        if k is None:
            return launch("last", carry[:SUBLANES, :LANES], carry)
        if callable(g):
            theirs = g(jnp.stack([chip, 1 - ci]).astype(jnp.int32), carry)
            held.append((k, theirs, _sibling_whole_copies))
            carried = launch(k, theirs[0, :SUBLANES, :LANES], carry)
            second_half[k] = functools.partial(g, chip_core, carried)
            return carried
        held.append((k, g, _sibling_copies))
        if k in ("w_down", "w_out", "w_attn_out"):
            return carry
        return launch(k, g[0, :SUBLANES, :LANES], carry)

    def finish(keys, after, tag):
        halves = []
        for k in keys:
            s32, st = pending[k]
            recv2 = _split_wait(f"chip_exchange_wait_{k}", st, _exchange_copies, after)[1]
            halves.append(_chip_sum_call(s32, recv2, chip_core, f"chip_sum_{k}"))
        return dict(zip(keys, _sibling_assemble_call(halves, f"grad_sibling_assemble_{tag}")))

    P = dict(b_gate=b_gate, rel_bias=rel_bias, conv_dw_b=conv_dw_b, conv_ln_g=conv_ln_g, conv_ln_b=conv_ln_b,
             norm_mix_pre=norm_mix_pre + launched, norm_mix_post=norm_mix_post, norm_ffn_pre=norm_ffn_pre,
             norm_ffn_post=norm_ffn_post, ffn_conv_b=ffn_conv_b)
    loss_tile, grad_x, G = _local_step(xs, target, P, late_weights, on_grad)

    small = [k for k in names if k not in big]
    packed = _pack([loss_tile[:1]] + [G[k] for k in small])
    (allsum,) = _split_start("small_grad_allsum_start",
                             [([packed], [jnp.zeros((N_DEVICES,) + packed.shape, F32)], N_DEVICES - 1, _allsum_copies)], core_arr)

    reduced, grads, deltas, new_m, new_v = {}, {}, {}, {}, {}

    def update(keys):
        for k in keys:
            gk, d, mn, vn = _adamw_call(weights[k][0], reduced[k], m_in[k][0], v_in[k][0], f"adamw_{k}")
            grads[k], deltas[k], new_m[k], new_v[k] = gk[None], d[None], mn[None], vn[None]

    others = [k for k in big if k != "w_in"]
    reduced.update(finish(others, allsum["tile"], "others"))
    update(others)
    reduced.update(finish(["w_in"], deltas["w_up"], "w_in"))
    update(["w_in"])

    me_chip = jnp.stack([4 * xi + 2 * yi + ci, chip]).astype(jnp.int32)
    mine, landed = _split_wait("small_grad_allsum_wait", allsum, _allsum_copies, deltas["w_in"])
    piece_shapes = [(1, LANES)] + [(G[k].size // LANES, LANES) if k == "rel_bias" else G[k].shape for k in small]
    piece_cols = [0] + [weights[k].shape[2] if k in ("conv_dw_w", "ffn_conv_w") else 0 for k in small]
    loss_row, *summed = _ordered_sum_call(mine, landed, me_chip, piece_shapes, piece_cols)
    loss = loss_row[0, 0]
    for k, gsum in zip(small, summed):
        grads[k] = gsum.reshape(weights[k].shape)
    ds, mns, vns = _adamw_small_call([weights[k] for k in small], [grads[k] for k in small],
                                     [m_in[k] for k in small], [v_in[k] for k in small])
    deltas.update(zip(small, ds))
    new_m.update(zip(small, mns))
    new_v.update(zip(small, vns))

    return (loss, grad_x[None], *[grads[k] for k in names], *[deltas[k] for k in names],
            *[new_m[k] for k in names], *[new_v[k] for k in names])
```

```python
import functools
import math

import jax
import jax.numpy as jnp
import numpy as np
from jax import lax
from jax.experimental import pallas as pl
from jax.experimental.pallas import tpu as pltpu

F32 = jnp.float32
BF16 = jnp.bfloat16
MESH = pl.DeviceIdType.MESH

HEAD_DIM = 128
HEADS_PER_GROUP = 4
DILATED_PATTERNS = ((128, 1), (512, 4), (2048, 16))
N_GROUPS = 3
N_HEADS = N_GROUPS * HEADS_PER_GROUP
SPAN = 128
GROUP_WIDTH = HEADS_PER_GROUP * HEAD_DIM
CONV_WIDTH = 31
FFN_CONV_WIDTH = 3
N_BUCKETS = 32
MAX_DISTANCE = 2048
RMS_EPS = 1e-6
LN_EPS = 1e-5
NEG_INF = -1e30
ADAM_LR = 0.001
ADAM_B1 = 0.9
ADAM_B2 = 0.999
ADAM_EPS = 1e-08
ADAM_WD = 0.01
ADAM_STEP = 10

LANES = 128
SUBLANES = 8
PACKED_ROWS = 16
ROW_TILE = 512
GATE_ROWS, GATE_COLS = 512, 512
TIME_BLOCK = 128
CONV_PAD = 32
FFN_PAD = 8
VMEM_LIMIT = 56 << 20


def _params(sem=None, vmem=None):
    kw = {}
    if sem is not None:
        kw["dimension_semantics"] = sem
    if vmem is not None:
        kw["vmem_limit_bytes"] = vmem
    return pltpu.CompilerParams(**kw)


def _pick(n, cands):
    for c in cands:
        if n % c == 0:
            return c
    return n


ELEMENTWISE_TILE_BYTES = 3 << 19


def _row_tile(rows, cols):
    for align in (16, SUBLANES):
        fits = [t for t in range(align, rows + 1, align) if rows % t == 0 and t * cols * 4 <= ELEMENTWISE_TILE_BYTES]
        if fits:
            return max(fits)
    return SUBLANES


N_CHIPS = 4
M_TILES = (1024, 1408, 512, 256, 128)
N_TILES = (1024, 512, 1408, 256, 128)
K_TILES = (2176, 2048, 1408, 1024, 512, 256, 128)


def _matmul(a, b, mode, name, out_shards=False, tm=None):
    assert a.dtype == BF16 and b.dtype == BF16, (name, a.dtype, b.dtype)
    b3 = b.ndim == 3
    tn = tk = None
    halves = None
    if mode == "nn":
        M, K = a.shape
        N = b.shape[-1] * (N_CHIPS if b3 else 1)
        tn = b.shape[-1] if b3 else None
    elif mode == "nt":
        if a.ndim == 3:
            halves = a.shape[2]
        M, K = a.shape[-2], a.shape[-1] * (a.shape[0] if a.ndim == 3 else 1)
        N = b.shape[-2]
        tk = b.shape[-1] if b3 else None
    else:
        if b3:
            halves = b.shape[2]
        K, M = a.shape
        N = b.shape[-1] * (b.shape[0] if b3 else 1)
        tn = N // N_CHIPS if out_shards else None
    tm = tm or _pick(M, M_TILES)
    tn = tn or _pick(N, N_TILES)
    tk = tk or _pick(K, K_TILES)
    nk = K // tk
    dn = {"nn": (((1,), (0,)), ((), ())), "nt": (((1,), (1,)), ((), ())), "tn": (((0,), (0,)), ((), ()))}[mode]

    def body(a_ref, b_ref, o_ref):
        if nk == 1:
            o_ref[...] = lax.dot_general(a_ref[...], b_ref[...], dn, preferred_element_type=F32)
        else:
            @pl.when(pl.program_id(2) == 0)
            def _():
                o_ref[...] = jnp.zeros_like(o_ref)

            o_ref[...] += lax.dot_general(a_ref[...], b_ref[...], dn, preferred_element_type=F32)

    if mode == "tn":
        a_spec = pl.BlockSpec((tk, tm), lambda i, j, k: (k, i))
    elif halves:
        per = halves // tk
        a_spec = pl.BlockSpec((None, tm, tk), lambda i, j, k: (k // per, i, k % per))
    else:
        a_spec = pl.BlockSpec((tm, tk), lambda i, j, k: (i, k))
    if mode == "nn":
        b_spec = pl.BlockSpec((None, tk, tn), lambda i, j, k: (j, k, 0)) if b3 else pl.BlockSpec((tk, tn), lambda i, j, k: (k, j))
    elif mode == "nt":
        b_spec = pl.BlockSpec((None, tn, tk), lambda i, j, k: (k, j, 0)) if b3 else pl.BlockSpec((tn, tk), lambda i, j, k: (j, k))
    elif halves:
        per = halves // tn
        b_spec = pl.BlockSpec((None, tk, tn), lambda i, j, k: (j // per, k, j % per))
    else:
        b_spec = pl.BlockSpec((tk, tn), lambda i, j, k: (k, j))
    if out_shards:
        out_spec = pl.BlockSpec((None, tm, tn), lambda i, j, k: (j, i, 0))
        out_shape = jax.ShapeDtypeStruct((N_CHIPS, M, tn), F32)
    else:
        out_spec = pl.BlockSpec((tm, tn), lambda i, j, k: (i, j))
        out_shape = jax.ShapeDtypeStruct((M, N), F32)
    return pl.pallas_call(
        body, name=name, grid=(M // tm, N // tn, nk),
        in_specs=[a_spec, b_spec], out_specs=out_spec, out_shape=out_shape,
        compiler_params=_params(("parallel", "parallel", "arbitrary"), VMEM_LIMIT),
    )(a, b)


def _grad_half_matmul(a, name, chip_half, b, init=None):
    K, M = a.shape
    parts = b.ndim == 3
    N = b.shape[-1] * (b.shape[0] if parts else 1)
    h, tn = M // 2, N // N_CHIPS
    summed = init is not None
    dn = (((0,), (0,)), ((), ()))

    def body(ch_ref, a_ref, b_ref, *rest):
        product = lax.dot_general(a_ref[...], b_ref[...], dn, preferred_element_type=F32)
        if not summed:
            rest[0][...] = product
            return
        init_ref, own_ref, sum16_ref = rest
        total = product + init_ref[...]
        sum16_ref[...] = total.astype(BF16)

        @pl.when(pl.program_id(0) == ch_ref[0])
        def _():
            own_ref[...] = total

    if parts:
        per = b.shape[2] // tn
        b_spec = pl.BlockSpec((None, K, tn), lambda j, ch_ref: (j // per, 0, j % per))
    else:
        b_spec = pl.BlockSpec((K, tn), lambda j, ch_ref: (0, j))
    shard_spec = pl.BlockSpec((None, h, tn), lambda j, ch_ref: (j, 0, 0))
    own_spec = pl.BlockSpec((h, tn), lambda j, ch_ref: (0, 0))
    shape = (N_CHIPS, h, tn)
    return pl.pallas_call(
        body, name=name + ("_mine" if summed else "_theirs"),
        grid_spec=pltpu.PrefetchScalarGridSpec(
            num_scalar_prefetch=1, grid=(N_CHIPS,),
            in_specs=[pl.BlockSpec((K, h), lambda j, ch_ref: (0, ch_ref[1])), b_spec] + [shard_spec] * summed,
            out_specs=[own_spec, shard_spec] if summed else shard_spec),
        out_shape=[jax.ShapeDtypeStruct((h, tn), F32), jax.ShapeDtypeStruct(shape, BF16)] if summed
        else jax.ShapeDtypeStruct(shape, F32),
        compiler_params=_params(("arbitrary",), VMEM_LIMIT),
    )(chip_half, a, b, *([init] if summed else []))


def _rms(x, g):
    r = lax.rsqrt(jnp.mean(x * x, axis=-1, keepdims=True) + RMS_EPS)
    return x * r * g


def _rms_bwd(x, g, dy):
    r = lax.rsqrt(jnp.mean(x * x, axis=-1, keepdims=True) + RMS_EPS)
    n = x * r
    dn = dy * g
    dx = r * (dn - n * jnp.mean(dn * n, axis=-1, keepdims=True))
    return dx, jnp.sum(dy * n, axis=0, keepdims=True)


def _sigmoid(x):
    return 1.0 / (1.0 + jnp.exp(-x))


_GELU_C = math.sqrt(2.0 / math.pi)


def _gelu(x):
    return 0.5 * x * (1.0 + jnp.tanh(_GELU_C * (x + 0.044715 * x * x * x)))


def _gelu_and_grad(x):
    x2 = x * x
    t = jnp.tanh(_GELU_C * x * (1.0 + 0.044715 * x2))
    half = 0.5 * (1.0 + t)
    return x * half, half + (0.5 * _GELU_C) * x * (1.0 - t * t) * (1.0 + (3.0 * 0.044715) * x2)


def _row_spec(width, col_block=0):
    return pl.BlockSpec((ROW_TILE, width), lambda i: (i, col_block))


def _vec_spec(width, col_block=0):
    return pl.BlockSpec((1, width), lambda i: (0, col_block))


def _accumulate(ref, part):
    @pl.when(pl.program_id(0) == 0)
    def _():
        ref[...] = part

    @pl.when(pl.program_id(0) > 0)
    def _():
        ref[...] += part


def _rms_fwd_call(x, g):
    S, D = x.shape

    def body(x_ref, g_ref, h_ref):
        h_ref[...] = _rms(x_ref[...], g_ref[...]).astype(BF16)

    return pl.pallas_call(
        body, name="rms_mix_pre", grid=(S // ROW_TILE,),
        in_specs=[_row_spec(D), _vec_spec(D)], out_specs=_row_spec(D),
        out_shape=jax.ShapeDtypeStruct((S, D), BF16),
        compiler_params=_params(("parallel",)),
    )(x, g)


def _ln_silu_call(c1, g, b):
    S, C = c1.shape

    def body(c_ref, g_ref, b_ref, o_ref):
        xv = c_ref[...]
        mu = jnp.mean(xv, axis=-1, keepdims=True)
        xc = xv - mu
        var = jnp.mean(xc * xc, axis=-1, keepdims=True)
        z = xc * lax.rsqrt(var + LN_EPS) * g_ref[...] + b_ref[...]
        o_ref[...] = (z * _sigmoid(z)).astype(BF16)

    return pl.pallas_call(
        body, name="conv_ln_silu", grid=(S // ROW_TILE,),
        in_specs=[_row_spec(C), _vec_spec(C), _vec_spec(C)], out_specs=_row_spec(C),
        out_shape=jax.ShapeDtypeStruct((S, C), BF16),
        compiler_params=_params(("parallel",)),
    )(c1, g, b)


def _ln_silu_bwd_call(c1, g, b, dc):
    S, C = c1.shape

    def body(c_ref, g_ref, b_ref, dc_ref, dx_ref, dg_ref, db_ref):
        xv = c_ref[...]
        mu = jnp.mean(xv, axis=-1, keepdims=True)
        xc = xv - mu
        rs = lax.rsqrt(jnp.mean(xc * xc, axis=-1, keepdims=True) + LN_EPS)
        xh = xc * rs
        z = xh * g_ref[...] + b_ref[...]
        sg = _sigmoid(z)
        dz = dc_ref[...] * (sg * (1.0 + z * (1.0 - sg)))
        dxh = dz * g_ref[...]
        dx_ref[...] = rs * (dxh - jnp.mean(dxh, axis=-1, keepdims=True) - xh * jnp.mean(dxh * xh, axis=-1, keepdims=True))
        _accumulate(dg_ref, jnp.sum(dz * xh, axis=0, keepdims=True))
        _accumulate(db_ref, jnp.sum(dz, axis=0, keepdims=True))

    return pl.pallas_call(
        body, name="conv_ln_silu_bwd", grid=(S // ROW_TILE,),
        in_specs=[_row_spec(C), _vec_spec(C), _vec_spec(C), _row_spec(C)],
        out_specs=[_row_spec(C), _vec_spec(C), _vec_spec(C)],
        out_shape=[jax.ShapeDtypeStruct((S, C), F32), jax.ShapeDtypeStruct((1, C), F32), jax.ShapeDtypeStruct((1, C), F32)],
        compiler_params=_params(("arbitrary",)),
    )(c1, g, b, dc)


def _mix_call(proj, gate_col0, b_gate, y_a, y_c):
    S, D = y_a.shape
    w = GATE_COLS
    nc = D // w
    ga0, gc0 = gate_col0 // w, (gate_col0 + D) // w

    def body(ga_ref, gc_ref, ba_ref, bc_ref, ya_ref, yc_ref, o_ref):
        o_ref[...] = (_sigmoid(ga_ref[...] + ba_ref[...]) * ya_ref[...]
                      + _sigmoid(gc_ref[...] + bc_ref[...]) * yc_ref[...]).astype(BF16)

    tile = lambda off: pl.BlockSpec((GATE_ROWS, w), lambda i, j: (i, off + j))
    vec = lambda off: pl.BlockSpec((1, w), lambda i, j: (0, off + j))
    return pl.pallas_call(
        body, name="gate_mix", grid=(S // GATE_ROWS, nc),
        in_specs=[tile(ga0), tile(gc0), vec(0), vec(nc), tile(0), tile(0)],
        out_specs=tile(0), out_shape=jax.ShapeDtypeStruct((S, D), BF16),
        compiler_params=_params(("parallel", "parallel")),
    )(proj, proj, b_gate, b_gate, y_a, y_c)


def _mix_bwd_call(dmixed, proj, gate_col0, b_gate, y_a, y_c):
    S, D = y_a.shape
    w = GATE_COLS
    nc = D // w
    ga0, gc0 = gate_col0 // w, (gate_col0 + D) // w

    def body(dm_ref, ga_ref, gc_ref, ba_ref, bc_ref, ya_ref, yc_ref, dya_ref, dyc_ref, dga_ref, dgc_ref, dba_ref, dbc_ref):
        dm = dm_ref[...]
        sa = _sigmoid(ga_ref[...] + ba_ref[...])
        sc = _sigmoid(gc_ref[...] + bc_ref[...])
        dya_ref[...] = (dm * sa).astype(BF16)
        dyc_ref[...] = (dm * sc).astype(BF16)
        dga = dm * ya_ref[...] * sa * (1.0 - sa)
        dgc = dm * yc_ref[...] * sc * (1.0 - sc)
        dga_ref[...] = dga.astype(BF16)
        dgc_ref[...] = dgc.astype(BF16)
        pa = jnp.sum(dga, axis=0, keepdims=True)
        pc = jnp.sum(dgc, axis=0, keepdims=True)

        @pl.when(pl.program_id(1) == 0)
        def _():
            dba_ref[...] = pa
            dbc_ref[...] = pc

        @pl.when(pl.program_id(1) > 0)
        def _():
            dba_ref[...] += pa
            dbc_ref[...] += pc

    tile = lambda off: pl.BlockSpec((GATE_ROWS, w), lambda j, i: (i, off + j))
    vec = lambda off: pl.BlockSpec((1, w), lambda j, i: (0, off + j))
    return pl.pallas_call(
        body, name="gate_mix_bwd", grid=(nc, S // GATE_ROWS),
        in_specs=[tile(0), tile(ga0), tile(gc0), vec(0), vec(nc), tile(0), tile(0)],
        out_specs=[tile(0), tile(0), tile(0), tile(0), vec(0), vec(0)],
        out_shape=[jax.ShapeDtypeStruct((S, D), BF16)] * 4 + [
                   jax.ShapeDtypeStruct((1, D), F32), jax.ShapeDtypeStruct((1, D), F32)],
        compiler_params=_params(("parallel", "arbitrary")),
    )(dmixed, proj, proj, b_gate, b_gate, y_a, y_c)


def _res1_call(x, out, g_post, g_pre):
    S, D = x.shape

    def body(x_ref, o_ref, gp_ref, gq_ref, x1_ref, h2_ref):
        x1 = x_ref[...] + _rms(o_ref[...], gp_ref[...])
        x1_ref[...] = x1
        h2_ref[...] = _rms(x1, gq_ref[...]).astype(BF16)

    return pl.pallas_call(
        body, name="residual_mix", grid=(S // ROW_TILE,),
        in_specs=[_row_spec(D), _row_spec(D), _vec_spec(D), _vec_spec(D)],
        out_specs=[_row_spec(D), _row_spec(D)],
        out_shape=[jax.ShapeDtypeStruct((S, D), F32), jax.ShapeDtypeStruct((S, D), BF16)],
        compiler_params=_params(("parallel",)),
    )(x, out, g_post, g_pre)


def _loss_call(y, x1, g_post, target):
    S, D = y.shape

    def body(y_ref, x1_ref, g_ref, t_ref, loss_ref, dx_ref, dy_ref, dg_ref):
        yv, gv = y_ref[...], g_ref[...]
        err = x1_ref[...] + _rms(yv, gv) - t_ref[...]
        dx2 = err * (1.0 / D)
        dx_ref[...] = dx2
        dy, dg = _rms_bwd(yv, gv, dx2)
        dy_ref[...] = dy.astype(BF16)
        _accumulate(dg_ref, dg)
        part = 0.5 * jnp.sum(jnp.mean(err * err, axis=-1, keepdims=True), axis=0, keepdims=True)
        _accumulate(loss_ref, jnp.broadcast_to(part, (SUBLANES, LANES)))

    return pl.pallas_call(
        body, name="residual_ffn_loss", grid=(S // ROW_TILE,),
        in_specs=[_row_spec(D), _row_spec(D), _vec_spec(D), _row_spec(D)],
        out_specs=[pl.BlockSpec((SUBLANES, LANES), lambda i: (0, 0)), _row_spec(D), _row_spec(D), _vec_spec(D)],
        out_shape=[jax.ShapeDtypeStruct((SUBLANES, LANES), F32), jax.ShapeDtypeStruct((S, D), F32),
                   jax.ShapeDtypeStruct((S, D), BF16), jax.ShapeDtypeStruct((1, D), F32)],
        compiler_params=_params(("arbitrary",)),
    )(y, x1, g_post, target)


def _mid_bwd_call(x1, g_pre, dh2, dx2, out, g_post):
    S, D = x1.shape

    def body(x1_ref, gq_ref, dh_ref, dx2_ref, o_ref, gp_ref, dx1_ref, do_ref, dgq_ref, dgp_ref):
        d, dgq = _rms_bwd(x1_ref[...], gq_ref[...], dh_ref[...])
        dx1 = dx2_ref[...] + d
        dx1_ref[...] = dx1
        do, dgp = _rms_bwd(o_ref[...], gp_ref[...], dx1)
        do_ref[...] = do.astype(BF16)
        _accumulate(dgq_ref, dgq)
        _accumulate(dgp_ref, dgp)

    return pl.pallas_call(
        body, name="residual_mix_bwd", grid=(S // ROW_TILE,),
        in_specs=[_row_spec(D), _vec_spec(D), _row_spec(D), _row_spec(D), _row_spec(D), _vec_spec(D)],
        out_specs=[_row_spec(D), _row_spec(D), _vec_spec(D), _vec_spec(D)],
        out_shape=[jax.ShapeDtypeStruct((S, D), F32), jax.ShapeDtypeStruct((S, D), BF16)] + [jax.ShapeDtypeStruct((1, D), F32)] * 2,
        compiler_params=_params(("arbitrary",)),
    )(x1, g_pre, dh2, dx2, out, g_post)


def _in_bwd_call(x, g, dh1, dx1):
    S, D = x.shape

    def body(x_ref, g_ref, dh_ref, dx1_ref, gx_ref, dg_ref):
        d, dg = _rms_bwd(x_ref[...], g_ref[...], dh_ref[...])
        gx_ref[...] = dx1_ref[...] + d
        _accumulate(dg_ref, dg)

    return pl.pallas_call(
        body, name="rms_mix_pre_bwd", grid=(S // ROW_TILE,),
        in_specs=[_row_spec(D), _vec_spec(D), _row_spec(D), _row_spec(D)],
        out_specs=[_row_spec(D), _vec_spec(D)],
        out_shape=[jax.ShapeDtypeStruct((S, D), F32), jax.ShapeDtypeStruct((1, D), F32)],
        compiler_params=_params(("arbitrary",)),
    )(x, g, dh1, dx1)


def _bucket_table(dilation):
    qi = np.arange(SPAN)[:, None]
    ki = np.arange(2 * SPAN)[None, :]
    dist = np.maximum(qi + SPAN - ki, 0) * dilation
    max_exact = N_BUCKETS // 2
    d = np.maximum(dist, 1).astype(np.float64)
    large = max_exact + (np.log(d / max_exact) / math.log(MAX_DISTANCE / max_exact) * (N_BUCKETS - max_exact)).astype(np.int32)
    large = np.minimum(large, N_BUCKETS - 1)
    return np.where(dist < max_exact, dist, large).astype(np.int32)


def _bucket_tables():
    return jnp.asarray(np.stack([_bucket_table(r) for _, r in DILATED_PATTERNS]))


def _bias_table_call(rel_bias, buckets):
    def body(rb_ref, bk_ref, o_ref):
        for h in range(N_HEADS):
            bk = bk_ref[h // HEADS_PER_GROUP]

            def step(b, acc):
                return jnp.where(bk == b, rb_ref[b, h], acc)

            o_ref[h] = lax.fori_loop(0, N_BUCKETS, step, jnp.zeros((SPAN, 2 * SPAN), F32))

    return pl.pallas_call(
        body, name="rel_bias_table",
        in_specs=[pl.BlockSpec(memory_space=pltpu.SMEM), pl.BlockSpec(memory_space=pltpu.VMEM)],
        out_specs=pl.BlockSpec(memory_space=pltpu.VMEM),
        out_shape=jax.ShapeDtypeStruct((N_HEADS, SPAN, 2 * SPAN), F32),
    )(rel_bias, buckets)


def _bias_grad_call(dbias, buckets):
    def body(db_ref, bk_ref, o_ref, rows_ref):
        for h in range(N_HEADS):
            bk = bk_ref[h // HEADS_PER_GROUP]
            dv = db_ref[h]

            def step(b, carry):
                rows_ref[h, b] = jnp.sum(jnp.where(bk == b, dv, 0.0), axis=0, keepdims=True)
                return carry

            lax.fori_loop(0, N_BUCKETS, step, 0)
        o_ref[...] = jnp.sum(rows_ref[...], axis=-1, keepdims=True)

    out = pl.pallas_call(
        body, name="rel_bias_grad",
        in_specs=[pl.BlockSpec(memory_space=pltpu.VMEM), pl.BlockSpec(memory_space=pltpu.VMEM)],
        out_specs=pl.BlockSpec(memory_space=pltpu.VMEM),
        out_shape=jax.ShapeDtypeStruct((N_HEADS, N_BUCKETS, 1, 1), F32),
        scratch_shapes=[pltpu.VMEM((N_HEADS, N_BUCKETS, 1, 2 * SPAN), F32)],
    )(dbias, buckets)
    return out.reshape(N_HEADS, N_BUCKETS).T


def _dot_nt(a, b):
    return lax.dot_general(a, b, (((1,), (1,)), ((), ())), preferred_element_type=F32)


def _dot_nn(a, b):
    return lax.dot_general(a, b, (((1,), (0,)), ((), ())), preferred_element_type=F32)


def _dot_tn(a, b):
    return lax.dot_general(a, b, (((0,), (0,)), ((), ())), preferred_element_type=F32)


def _band_masks(n, nb):
    qi = lax.broadcasted_iota(jnp.int32, (SPAN, SPAN), 0)
    ki = lax.broadcasted_iota(jnp.int32, (SPAN, SPAN), 1)
    prev_ok = jnp.logical_and(ki >= qi, n > 0)
    cur_ok = ki <= qi
    next_ok = jnp.logical_and(ki >= qi, n < nb - 1)
    return prev_ok, cur_ok, next_ok


def _wide_band_mask(n):
    qi = lax.broadcasted_iota(jnp.int32, (SPAN, 2 * SPAN), 0)
    ki = lax.broadcasted_iota(jnp.int32, (SPAN, 2 * SPAN), 1)
    prev_ok = jnp.logical_and(jnp.logical_and(ki < SPAN, ki >= qi), n > 0)
    cur_ok = jnp.logical_and(ki >= SPAN, ki - SPAN <= qi)
    return jnp.logical_or(prev_ok, cur_ok)


def _attn_plan(S, group):
    r = DILATED_PATTERNS[group][1]
    hp, per = (HEADS_PER_GROUP, 1) if r == 1 else (2, 4)
    return r, S // (r * SPAN), hp, per


def _residue_rows(rho, r):
    return slice(None) if r == 1 else pl.ds(rho, SPAN, stride=r)


def _for_residues(r, per, fn):
    if r == per:
        for u in range(per):
            fn(u)
        return

    def step(i, carry):
        for u in range(per):
            fn(i * per + u)
        return carry

    lax.fori_loop(0, r // per, step, 0)


def _attn_fwd_call(proj, bias, group):
    S = proj.shape[0]
    r, nb, hp, per = _attn_plan(S, group)
    scale = HEAD_DIM ** -0.5
    kinds = ("q", "kp", "kc", "vp", "vc") if nb > 1 else ("q", "kc", "vc")

    per_kind = _refs_per_kind(r, hp)

    def body(*refs):
        ins = {kind: refs[i * per_kind:(i + 1) * per_kind] for i, kind in enumerate(kinds)}
        b_ref, o_ref, lse_ref = refs[len(kinds) * per_kind:]
        n = pl.program_id(1)
        prev_ok, cur_ok, _ = _band_masks(n, nb)

        band_ok = _wide_band_mask(n) if nb > 1 else cur_ok

        def residue(rho):
            rows = _residue_rows(rho, r)
            for j in range(hp):
                get = lambda kind: _head_rows(ins[kind], j, rows, r).astype(BF16)
                q = get("q")
                if nb > 1:
                    keys, vals, bias_j = jnp.concatenate([get("kp"), get("kc")], axis=0), jnp.concatenate([get("vp"), get("vc")], axis=0), b_ref[j]
                else:
                    keys, vals, bias_j = get("kc"), get("vc"), b_ref[j, :, SPAN:]
                s = jnp.where(band_ok, _dot_nt(q, keys) * scale + bias_j, NEG_INF)
                m = jnp.max(s, axis=-1, keepdims=True)
                p = jnp.exp(s - m)
                den = jnp.sum(p, axis=-1, keepdims=True)
                o_ref[j, rows, :] = _dot_nn(p.astype(BF16), vals) / den
                lse_ref[j, rows, :] = jnp.broadcast_to(m + jnp.log(den), (SPAN, HEAD_DIM))

        _for_residues(r, per, residue)

    in_specs = [_head_spec(r, nb, hp, kind, group, jj) for kind in kinds for jj in range(per_kind)]
    in_specs.append(pl.BlockSpec((hp, SPAN, 2 * SPAN), lambda j, n: (group * (HEADS_PER_GROUP // hp) + j, 0, 0)))
    out = pl.BlockSpec((hp, r * SPAN, HEAD_DIM), lambda j, n: (j, n, 0))
    return pl.pallas_call(
        body, name=f"attn_fwd_g{group}", grid=(HEADS_PER_GROUP // hp, nb),
        in_specs=in_specs, out_specs=[out] * 2,
        out_shape=[jax.ShapeDtypeStruct((HEADS_PER_GROUP, S, HEAD_DIM), F32)] * 2,
        compiler_params=_params(("parallel", "parallel"), VMEM_LIMIT),
    )(*([proj] * (len(in_specs) - 1)), bias)


_PROJ_PART = dict(q=0, qn=0, kp=1, kc=1, vp=2, vc=2)


def _refs_per_kind(r, hp):
    return 1 if r == 1 else hp


def _head_rows(refs, j, rows, r):
    return refs[0][:, j * HEAD_DIM:(j + 1) * HEAD_DIM] if r == 1 else refs[j][rows, :]


def _head_spec(r, nb, hp, kind, group, jj):
    if kind in _PROJ_PART:
        base = (_PROJ_PART[kind] * N_GROUPS + group) * HEADS_PER_GROUP
    else:
        base = 0
    if kind.endswith("p"):
        row = lambda n: jnp.maximum(n - 1, 0)
    elif kind.endswith("n"):
        row = lambda n: jnp.minimum(n + 1, nb - 1)
    else:
        row = lambda n: n
    if r == 1:
        return pl.BlockSpec((SPAN, hp * HEAD_DIM), lambda j, n: (row(n), base // hp + j))
    return pl.BlockSpec((r * SPAN, HEAD_DIM), lambda j, n: (row(n), base + j * hp + jj))


def _attn_merge_call(parts):
    S = parts[0].shape[1]

    def body(o1, s1, o2, s2, o3, s3, a_ref, ab_ref, lse_ref):
        for j in range(HEADS_PER_GROUP):
            sl = slice(j * HEAD_DIM, (j + 1) * HEAD_DIM)
            mx = jnp.maximum(jnp.maximum(s1[j], s2[j]), s3[j])
            w1 = jnp.exp(s1[j] - mx)
            w2 = jnp.exp(s2[j] - mx)
            w3 = jnp.exp(s3[j] - mx)
            den = w1 + w2 + w3
            a = (w1 * o1[j] + w2 * o2[j] + w3 * o3[j]) / den
            a_ref[:, sl] = a
            ab_ref[:, sl] = a.astype(BF16)
            lse_ref[:, sl] = mx + jnp.log(den)

    heads = pl.BlockSpec((HEADS_PER_GROUP, ROW_TILE, HEAD_DIM), lambda i: (0, i, 0))
    return pl.pallas_call(
        body, name="attn_merge", grid=(S // ROW_TILE,),
        in_specs=[heads] * 6, out_specs=[_row_spec(GROUP_WIDTH)] * 3,
        out_shape=[jax.ShapeDtypeStruct((S, GROUP_WIDTH), F32), jax.ShapeDtypeStruct((S, GROUP_WIDTH), BF16),
                   jax.ShapeDtypeStruct((S, GROUP_WIDTH), F32)],
        compiler_params=_params(("parallel",)),
    )(*parts)


def _attn_delta_call(a, da):
    S = a.shape[0]

    def body(a_ref, da_ref, d_ref):
        for j in range(HEADS_PER_GROUP):
            sl = slice(j * HEAD_DIM, (j + 1) * HEAD_DIM)
            d = jnp.sum(a_ref[:, sl] * da_ref[:, sl], axis=-1, keepdims=True)
            d_ref[:, sl] = jnp.broadcast_to(d, (ROW_TILE, HEAD_DIM))

    return pl.pallas_call(
        body, name="attn_delta", grid=(S // ROW_TILE,),
        in_specs=[_row_spec(GROUP_WIDTH)] * 2, out_specs=_row_spec(GROUP_WIDTH),
        out_shape=jax.ShapeDtypeStruct((S, GROUP_WIDTH), F32),
        compiler_params=_params(("parallel",)),
    )(a, da)


def _attn_bwd_call(proj, bias, da, lse, delta, group):
    S = proj.shape[0]
    r, nb, hp, per = _attn_plan(S, group)
    scale = HEAD_DIM ** -0.5
    kinds = ("q", "qn", "kp", "kc", "vp", "vc", "da", "dan", "lse", "lsen", "dl", "dln") if nb > 1 else ("q", "kc", "vc", "da", "lse", "dl")
    source = dict(da=da, dan=da, lse=lse, lsen=lse, dl=delta, dln=delta)

    per_kind = _refs_per_kind(r, hp)

    def body(*refs):
        ins = {kind: refs[i * per_kind:(i + 1) * per_kind] for i, kind in enumerate(kinds)}
        b_ref, dq_ref, dk_ref, dv_ref, db_ref = refs[len(kinds) * per_kind:]
        n = pl.program_id(1)
        prev_ok, cur_ok, next_ok = _band_masks(n, nb)

        @pl.when(n == 0)
        def _():
            db_ref[...] = jnp.zeros_like(db_ref)

        band_ok = _wide_band_mask(n) if nb > 1 else cur_ok

        def residue(rho):
            rows = _residue_rows(rho, r)
            for j in range(hp):
                get = lambda kind: _head_rows(ins[kind], j, rows, r)
                q = get("q").astype(BF16)
                kc = get("kc").astype(BF16)
                vc = get("vc").astype(BF16)
                dav = get("da").astype(BF16)
                lse_q, dl_q = get("lse"), get("dl")
                if nb == 1:
                    pc = jnp.exp(jnp.where(cur_ok, _dot_nt(q, kc) * scale + b_ref[j, :, SPAN:], NEG_INF) - lse_q)
                    dsc = pc * (_dot_nt(dav, vc) - dl_q)
                    dsc_b = dsc.astype(BF16)
                    dq = _dot_nn(dsc_b, kc)
                    dk = _dot_tn(dsc_b, q)
                    dv = _dot_tn(pc.astype(BF16), dav)
                    db_ref[j, :, SPAN:] += dsc
                else:
                    qn = get("qn").astype(BF16)
                    dan = get("dan").astype(BF16)
                    keys = jnp.concatenate([get("kp").astype(BF16), kc], axis=0)
                    vals = jnp.concatenate([get("vp").astype(BF16), vc], axis=0)
                    wide = lambda t: jnp.concatenate([t, t], axis=1)
                    p = jnp.exp(jnp.where(band_ok, _dot_nt(q, keys) * scale + b_ref[j], NEG_INF) - wide(lse_q))
                    ds = p * (_dot_nt(dav, vals) - wide(dl_q))
                    dq = _dot_nn(ds.astype(BF16), keys)
                    db_ref[j] += ds
                    pn = jnp.exp(jnp.where(next_ok, _dot_nt(qn, kc) * scale + b_ref[j, :, :SPAN], NEG_INF) - get("lsen"))
                    dsn = pn * (_dot_nt(dan, vc) - get("dln"))
                    both = lambda cur_part, next_part: jnp.concatenate([cur_part.astype(BF16), next_part.astype(BF16)], axis=0)
                    dk = _dot_tn(both(ds[:, SPAN:], dsn), jnp.concatenate([q, qn], axis=0))
                    dv = _dot_tn(both(p[:, SPAN:], pn), jnp.concatenate([dav, dan], axis=0))
                dq_ref[j, rows, :] = dq * scale
                dk_ref[j, rows, :] = dk * scale
                dv_ref[j, rows, :] = dv

        _for_residues(r, per, residue)

    per_group = HEADS_PER_GROUP // hp
    band = (hp, SPAN, 2 * SPAN)
    in_specs = [_head_spec(r, nb, hp, kind, group, jj) for kind in kinds for jj in range(per_kind)]
    in_specs.append(pl.BlockSpec(band, lambda j, n: (group * per_group + j, 0, 0)))
    operands = [source.get(kind, proj) for kind in kinds for _ in range(per_kind)] + [bias]
    out = pl.BlockSpec((hp, r * SPAN, HEAD_DIM), lambda j, n: (j, n, 0))
    return pl.pallas_call(
        body, name=f"attn_bwd_g{group}", grid=(per_group, nb),
        in_specs=in_specs,
        out_specs=[out] * 3 + [pl.BlockSpec(band, lambda j, n: (j, 0, 0))],
        out_shape=[jax.ShapeDtypeStruct((HEADS_PER_GROUP, S, HEAD_DIM), F32)] * 3
        + [jax.ShapeDtypeStruct((HEADS_PER_GROUP, SPAN, 2 * SPAN), F32)],
        compiler_params=_params(("parallel", "arbitrary"), VMEM_LIMIT),
    )(*operands)


def _dproj_call(dqkv, tails):
    S = tails[0].shape[0]
    width = len(dqkv) * GROUP_WIDTH + sum(t.shape[1] for t in tails)

    def body(*refs):
        o_ref = refs[-1]
        col = 0
        for ref in refs[:len(dqkv)]:
            for j in range(HEADS_PER_GROUP):
                o_ref[:, col:col + HEAD_DIM] = ref[j].astype(BF16)
                col += HEAD_DIM
        for ref in refs[len(dqkv):-1]:
            o_ref[:, col:col + ref.shape[1]] = ref[...]
            col += ref.shape[1]

    heads = pl.BlockSpec((HEADS_PER_GROUP, ROW_TILE, HEAD_DIM), lambda i: (0, i, 0))
    return pl.pallas_call(
        body, name="dproj_assemble", grid=(S // ROW_TILE,),
        in_specs=[heads] * len(dqkv) + [_row_spec(t.shape[1]) for t in tails],
        out_specs=_row_spec(width), out_shape=jax.ShapeDtypeStruct((S, width), BF16),
        compiler_params=_params(("parallel",)),
    )(*dqkv, *tails)


def _tap_rows(xpad_ref, t0, k, width, pad):
    return xpad_ref[pl.ds(t0 + (pad - (width - 1 - k)), TIME_BLOCK), :]


def _conv_block(xpad_ref, t0, w_ref, width, pad):
    acc = None
    for k in range(width):
        term = w_ref[k:k + 1, :] * _tap_rows(xpad_ref, t0, k, width, pad)
        acc = term if acc is None else acc + term
    return acc


def _conv_transpose_block(dpad_ref, t0, w_ref, width):
    acc = None
    for k in range(width):
        term = w_ref[k:k + 1, :] * dpad_ref[pl.ds(t0 + (width - 1 - k), TIME_BLOCK), :]
        acc = term if acc is None else acc + term
    return acc


def _conv_weight_grad(xpad_ref, t0, dy, dw_ref, width, pad):
    for k in range(width):
        dw_ref[k:k + 1, :] += jnp.sum(dy * _tap_rows(xpad_ref, t0, k, width, pad), axis=0, keepdims=True)


def _time_loop(S, step, skip_first=0, skip_last=0):
    def it(tb, carry):
        step(pl.multiple_of(tb * TIME_BLOCK, TIME_BLOCK))
        return carry

    lax.fori_loop(skip_first, S // TIME_BLOCK - skip_last, it, 0)


def _fill_head(head_ref, x_ref, pad):
    head_ref[0:pad, :] = jnp.zeros((pad, LANES), F32)
    head_ref[pad:, :] = x_ref[0:TIME_BLOCK, :]


def _fill_tail(tail_ref, x_ref, pad):
    S = x_ref.shape[0]
    tail_ref[0:TIME_BLOCK, :] = x_ref[S - TIME_BLOCK:S, :]
    tail_ref[TIME_BLOCK:, :] = jnp.zeros((pad, LANES), F32)


def _conv_fwd_call(proj, col0, w, b):
    S = proj.shape[0]
    C = w.shape[1]
    nt = C // LANES
    v0, g0 = col0 // LANES, (col0 + C) // LANES

    def body(val_ref, gate_ref, w_ref, b_ref, o_ref, pad_ref):
        pad_ref[0:CONV_PAD, :] = jnp.zeros((CONV_PAD, LANES), F32)
        pad_ref[CONV_PAD:, :] = val_ref[...] * _sigmoid(gate_ref[...])

        def step(t0):
            o_ref[pl.ds(t0, TIME_BLOCK), :] = _conv_block(pad_ref, t0, w_ref, CONV_WIDTH, CONV_PAD) + b_ref[...]

        _time_loop(S, step)

    seq = lambda off: pl.BlockSpec((S, LANES), lambda i: (0, off + i))
    return pl.pallas_call(
        body, name="conv_module", grid=(nt,),
        in_specs=[seq(v0), seq(g0), pl.BlockSpec((CONV_WIDTH, LANES), lambda i: (0, i)), pl.BlockSpec((1, LANES), lambda i: (0, i))],
        out_specs=seq(0), out_shape=jax.ShapeDtypeStruct((S, C), F32),
        scratch_shapes=[pltpu.VMEM((S + CONV_PAD, LANES), F32)],
        compiler_params=_params(("parallel",)),
    )(proj, proj, w, b)


def _conv_bwd_call(proj, col0, w, dc1):
    S = proj.shape[0]
    C = w.shape[1]
    nt = C // LANES
    v0, g0 = col0 // LANES, (col0 + C) // LANES

    def body(val_ref, gate_ref, w_ref, dy_ref, dval_ref, dgate_ref, dw_ref, db_ref, xpad_ref, tail_ref, dwacc_ref):
        xpad_ref[0:CONV_PAD, :] = jnp.zeros((CONV_PAD, LANES), F32)
        xpad_ref[CONV_PAD:, :] = val_ref[...] * _sigmoid(gate_ref[...])
        _fill_tail(tail_ref, dy_ref, CONV_PAD)
        dwacc_ref[...] = jnp.zeros_like(dwacc_ref)

        def block(t0, dy_src, dy_t0):
            rows = pl.ds(t0, TIME_BLOCK)
            _conv_weight_grad(xpad_ref, t0, dy_ref[rows, :], dwacc_ref, CONV_WIDTH, CONV_PAD)
            dc0 = _conv_transpose_block(dy_src, dy_t0, w_ref, CONV_WIDTH)
            sg = _sigmoid(gate_ref[rows, :])
            dval_ref[rows, :] = (dc0 * sg).astype(BF16)
            dgate_ref[rows, :] = (dc0 * val_ref[rows, :] * sg * (1.0 - sg)).astype(BF16)

        _time_loop(S, lambda t0: block(t0, dy_ref, t0), skip_last=1)
        block(S - TIME_BLOCK, tail_ref, 0)
        dw_ref[...] = dwacc_ref[...]
        db_ref[...] = jnp.sum(dy_ref[...], axis=0, keepdims=True)

    seq = lambda off: pl.BlockSpec((S, LANES), lambda i: (0, off + i))
    return pl.pallas_call(
        body, name="conv_module_bwd", grid=(nt,),
        in_specs=[seq(v0), seq(g0), pl.BlockSpec((CONV_WIDTH, LANES), lambda i: (0, i)), seq(0)],
        out_specs=[seq(0), seq(0), pl.BlockSpec((CONV_PAD, LANES), lambda i: (0, i)), pl.BlockSpec((1, LANES), lambda i: (0, i))],
        out_shape=[jax.ShapeDtypeStruct((S, C), BF16), jax.ShapeDtypeStruct((S, C), BF16),
                   jax.ShapeDtypeStruct((CONV_PAD, C), F32), jax.ShapeDtypeStruct((1, C), F32)],
        scratch_shapes=[pltpu.VMEM((S + CONV_PAD, LANES), F32), pltpu.VMEM((TIME_BLOCK + CONV_PAD, LANES), F32),
                        pltpu.VMEM((CONV_PAD, LANES), F32)],
        compiler_params=_params(("parallel",)),
    )(proj, proj, w, dc1)


def _ffn_fwd_call(u, w, b):
    S, C2 = u.shape
    C = C2 // 2
    nt = C // LANES

    def body(ug_ref, uv_ref, wg_ref, wv_ref, bg_ref, bv_ref, f_ref, hg_ref, hv_ref):
        _fill_head(hg_ref, ug_ref, FFN_PAD)
        _fill_head(hv_ref, uv_ref, FFN_PAD)

        def block(t0, xg_ref, xv_ref, x_t0, pad):
            cg = _conv_block(xg_ref, x_t0, wg_ref, FFN_CONV_WIDTH, pad) + bg_ref[...]
            cv = _conv_block(xv_ref, x_t0, wv_ref, FFN_CONV_WIDTH, pad) + bv_ref[...]
            f_ref[pl.ds(t0, TIME_BLOCK), :] = (_gelu(cg) * cv).astype(BF16)

        block(0, hg_ref, hv_ref, 0, FFN_PAD)
        _time_loop(S, lambda t0: block(t0, ug_ref, uv_ref, t0, 0), skip_first=1)

    seq = lambda off: pl.BlockSpec((S, LANES), lambda i: (0, off + i))
    wsp = lambda off: pl.BlockSpec((FFN_CONV_WIDTH, LANES), lambda i: (0, off + i))
    bsp = lambda off: pl.BlockSpec((1, LANES), lambda i: (0, off + i))
    return pl.pallas_call(
        body, name="ffn_conv_geglu", grid=(nt,),
        in_specs=[seq(0), seq(nt), wsp(0), wsp(nt), bsp(0), bsp(nt)],
        out_specs=seq(0), out_shape=jax.ShapeDtypeStruct((S, C), BF16),
        scratch_shapes=[pltpu.VMEM((FFN_PAD + TIME_BLOCK, LANES), F32)] * 2,
        compiler_params=_params(("parallel",)),
    )(u, u, w, w, b, b)


def _ffn_bwd_call(u, w, b, df):
    S, C2 = u.shape
    C = C2 // 2
    nt = C // LANES

    def body(ug_ref, uv_ref, wg_ref, wv_ref, bg_ref, bv_ref, df_ref,
             du_ref, dwg_ref, dwv_ref, dbg_ref, dbv_ref,
             hg_ref, hv_ref, dg_ref, dv_ref, dwg_acc, dwv_acc, dbg_acc, dbv_acc):
        zeros = jnp.zeros((FFN_PAD, LANES), F32)
        _fill_head(hg_ref, ug_ref, FFN_PAD)
        _fill_head(hv_ref, uv_ref, FFN_PAD)
        dg_ref[S:, :] = zeros
        dv_ref[S:, :] = zeros
        dwg_acc[...] = jnp.zeros_like(dwg_acc)
        dwv_acc[...] = jnp.zeros_like(dwv_acc)
        dbg_acc[...] = jnp.zeros_like(dbg_acc)
        dbv_acc[...] = jnp.zeros_like(dbv_acc)

        def first(t0, xg_ref, xv_ref, x_t0, pad):
            rows = pl.ds(t0, TIME_BLOCK)
            cg = _conv_block(xg_ref, x_t0, wg_ref, FFN_CONV_WIDTH, pad) + bg_ref[...]
            cv = _conv_block(xv_ref, x_t0, wv_ref, FFN_CONV_WIDTH, pad) + bv_ref[...]
            dfb = df_ref[rows, :]
            gelu, gelu_grad = _gelu_and_grad(cg)
            dcg = dfb * cv * gelu_grad
            dcv = dfb * gelu
            dg_ref[rows, :] = dcg
            dv_ref[rows, :] = dcv
            _conv_weight_grad(xg_ref, x_t0, dcg, dwg_acc, FFN_CONV_WIDTH, pad)
            _conv_weight_grad(xv_ref, x_t0, dcv, dwv_acc, FFN_CONV_WIDTH, pad)
            dbg_acc[...] += jnp.sum(dcg, axis=0, keepdims=True)
            dbv_acc[...] += jnp.sum(dcv, axis=0, keepdims=True)

        def second(t0):
            rows = pl.ds(t0, TIME_BLOCK)
            du_ref[0, rows, :] = _conv_transpose_block(dg_ref, t0, wg_ref, FFN_CONV_WIDTH).astype(BF16)
            du_ref[1, rows, :] = _conv_transpose_block(dv_ref, t0, wv_ref, FFN_CONV_WIDTH).astype(BF16)

        first(0, hg_ref, hv_ref, 0, FFN_PAD)
        _time_loop(S, lambda t0: first(t0, ug_ref, uv_ref, t0, 0), skip_first=1)
        _time_loop(S, second)
        dwg_ref[...] = dwg_acc[...]
        dwv_ref[...] = dwv_acc[...]
        dbg_ref[...] = dbg_acc[...]
        dbv_ref[...] = dbv_acc[...]

    seq = lambda off: pl.BlockSpec((S, LANES), lambda i: (0, off + i))
    wsp = lambda off: pl.BlockSpec((FFN_CONV_WIDTH, LANES), lambda i: (0, off + i))
    bsp = lambda off: pl.BlockSpec((1, LANES), lambda i: (0, off + i))
    return pl.pallas_call(
        body, name="ffn_conv_geglu_bwd", grid=(nt,),
        in_specs=[seq(0), seq(nt), wsp(0), wsp(nt), bsp(0), bsp(nt), seq(0)],
        out_specs=[pl.BlockSpec((2, S, LANES), lambda i: (0, 0, i)),
                   pl.BlockSpec((SUBLANES, LANES), lambda i: (0, i)), pl.BlockSpec((SUBLANES, LANES), lambda i: (0, i)),
                   bsp(0), bsp(0)],
        out_shape=[jax.ShapeDtypeStruct((2, S, C), BF16)] + [jax.ShapeDtypeStruct((SUBLANES, C), F32)] * 2
        + [jax.ShapeDtypeStruct((1, C), F32)] * 2,
        scratch_shapes=[pltpu.VMEM((FFN_PAD + TIME_BLOCK, LANES), F32)] * 2 + [pltpu.VMEM((S + FFN_PAD, LANES), F32)] * 2
        + [pltpu.VMEM((SUBLANES, LANES), F32)] * 2
        + [pltpu.VMEM((1, LANES), F32)] * 2,
        compiler_params=_params(("parallel",)),
    )(u, u, w, w, b, b, df)


def _adamw(w_ref, g_ref, m_ref, v_ref, d_ref, mo_ref, vo_ref):
    gv = g_ref[...]
    mn = ADAM_B1 * m_ref[...] + (1.0 - ADAM_B1) * gv
    vn = ADAM_B2 * v_ref[...] + (1.0 - ADAM_B2) * (gv * gv)
    mo_ref[...] = mn
    vo_ref[...] = vn
    m_hat = mn * (1.0 / (1.0 - ADAM_B1 ** ADAM_STEP))
    v_hat = vn * (1.0 / (1.0 - ADAM_B2 ** ADAM_STEP))
    d_ref[...] = -ADAM_LR * (m_hat / (jnp.sqrt(v_hat) + ADAM_EPS) + ADAM_WD * w_ref[...])


def _adamw_call(w, g, m, v, name):
    R, C = w.shape
    tr = _row_tile(R, C)

    def body(w_ref, g_ref, m_ref, v_ref, go_ref, d_ref, mo_ref, vo_ref):
        go_ref[...] = g_ref[...]
        _adamw(w_ref, g_ref, m_ref, v_ref, d_ref, mo_ref, vo_ref)

    spec = pl.BlockSpec((tr, C), lambda i: (i, 0))
    return pl.pallas_call(
        body, name=name, grid=(R // tr,),
        in_specs=[spec] * 4, out_specs=[spec] * 4,
        out_shape=[jax.ShapeDtypeStruct((R, C), F32)] * 4,
        compiler_params=_params(("parallel",)),
    )(w, g, m, v)


def _adamw_small_call(ws, gs, ms, vs):
    n = len(ws)

    def body(*refs):
        w_refs, g_refs, m_refs, v_refs, d_refs, mo_refs, vo_refs = (refs[i * n:(i + 1) * n] for i in range(7))
        for i in range(n):
            _adamw(w_refs[i], g_refs[i], m_refs[i], v_refs[i], d_refs[i], mo_refs[i], vo_refs[i])

    whole = pl.BlockSpec(memory_space=pltpu.VMEM)
    outs = pl.pallas_call(
        body, name="adamw_small",
        in_specs=[whole] * (4 * n), out_specs=[whole] * (3 * n),
        out_shape=[jax.ShapeDtypeStruct(w.shape, F32) for w in ws] * 3,
    )(*ws, *gs, *ms, *vs)
    return outs[:n], outs[n:2 * n], outs[2 * n:]


def _position():
    return lax.axis_index("x"), lax.axis_index("y"), lax.axis_index("c")


def _chip_peers(x, y):
    return [(x, 1 - y), (1 - x, y), (1 - x, 1 - y)]


def _half_rows(ref, core, rows):
    h = rows // 2
    start = pl.multiple_of(core * h, PACKED_ROWS)
    return ref.at[pl.ds(start, h), :] if len(ref.shape) == 2 else ref.at[:, pl.ds(start, h), :]


def _shard_half(ref, shard, core, rows):
    h = rows // 2
    return ref.at[shard, pl.ds(pl.multiple_of(core * h, PACKED_ROWS), h), :]


ANY = pl.BlockSpec(memory_space=pl.ANY)


def _first_hop_copies(srcs, lands):
    x, y, c = _position()
    chip = 2 * x + y
    targets = [(px, py, c) for px, py in _chip_peers(x, y)] + [(x, y, 1 - c)]
    rows = srcs[0].shape[0]
    out = []
    for i, (s, l) in enumerate(zip(srcs, lands)):
        for k, dev in enumerate(targets):
            if i == 0 and k < 3:
                out.append((_half_rows(s, c, rows), _shard_half(l, chip, c, rows), dev, k))
            else:
                out.append((s, l.at[chip], dev, len(targets) * i + k))
    return out


def _second_hop_copies(srcs, lands):
    x, y, c = _position()
    rows = lands[0].shape[1]
    out = []
    for k, (px, py) in enumerate(_chip_peers(x, y)):
        half = _shard_half(lands[0], 2 * px + py, c, rows)
        out.append((half, half, (x, y, 1 - c), k))
    return out


HBM_SPEC = pl.BlockSpec(memory_space=pltpu.HBM)
SEM_SPEC = pl.BlockSpec(memory_space=pltpu.SEMAPHORE)
DATAFLOW = pltpu.SideEffectType.DATAFLOW_SIDE_EFFECTING


def _in_hbm(a):
    return pltpu.with_memory_space_constraint(a, pltpu.HBM)


def _split_start(name, groups, after, carry=None):
    spans, arrays = [], []
    for srcs, lands, _, _ in groups:
        spans.append((len(arrays), len(srcs), len(lands)))
        arrays += list(srcs) + list(lands)
    if carry is not None:
        arrays.append(carry)
    na, ng = len(arrays), len(groups)

    def body(*refs):
        sems, token = refs[na + 1:na + 1 + 2 * ng], refs[-1]
        for g, (_, _, _, copies) in enumerate(groups):
            off, ns, nl = spans[g]
            for src, dst, dev, idx in copies(refs[off:off + ns], refs[off + ns:off + ns + nl]):
                pltpu.make_async_remote_copy(src_ref=src, dst_ref=dst, send_sem=sems[2 * g].at[idx], recv_sem=sems[2 * g + 1].at[idx],
                                             device_id=dev, device_id_type=MESH).start()
        token[...] = jnp.zeros_like(token)

    outs = pl.pallas_call(
        body, name=name,
        in_specs=[HBM_SPEC] * na + [ANY],
        out_specs=[SEM_SPEC] * (2 * ng) + [HBM_SPEC] * na + [pl.BlockSpec(memory_space=pltpu.VMEM)],
        out_shape=[pltpu.SemaphoreType.DMA((n_sems,)) for _, _, n_sems, _ in groups for _ in range(2)]
        + [pltpu.HBM(a.shape, a.dtype) for a in arrays] + [jax.ShapeDtypeStruct((SUBLANES, LANES), F32)],
        input_output_aliases={i: 2 * ng + i for i in range(na)},
        compiler_params=pltpu.CompilerParams(has_side_effects=DATAFLOW),
    )(*[_in_hbm(a) for a in arrays], after)
    started = []
    for g, (off, ns, nl) in enumerate(spans):
        thru = outs[2 * ng + off:2 * ng + off + ns + nl]
        started.append(dict(send=outs[2 * g], recv=outs[2 * g + 1], srcs=list(thru[:ns]), lands=list(thru[ns:]),
                            tile=outs[-1], token=outs[-1][0, 0], carry=None if carry is None else outs[2 * ng + na - 1]))
    return started


def _split_wait(name, started, copies, after):
    n, m = len(started["srcs"]), len(started["lands"])

    def body(*refs):
        src_refs, land_refs = refs[:n], refs[n:n + m]
        send_sem, recv_sem = refs[n + m], refs[n + m + 1]
        for src, dst, dev, idx in copies(src_refs, land_refs):
            cp = pltpu.make_async_remote_copy(src_ref=src, dst_ref=dst, send_sem=send_sem.at[idx], recv_sem=recv_sem.at[idx],
                                              device_id=dev, device_id_type=MESH)
            cp.wait_send()
            cp.wait_recv()

    arrays = started["srcs"] + started["lands"]
    outs = pl.pallas_call(
        body, name=name,
        in_specs=[HBM_SPEC] * (n + m) + [SEM_SPEC, SEM_SPEC, ANY],
        out_specs=[HBM_SPEC] * (n + m),
        out_shape=[pltpu.HBM(a.shape, a.dtype) for a in arrays],
        input_output_aliases={i: i for i in range(n + m)},
        compiler_params=pltpu.CompilerParams(has_side_effects=DATAFLOW),
    )(*arrays, started["send"], started["recv"], after)
    return list(outs)


def _gather_copies(srcs, lands):
    x, y, c = _position()
    chip = 2 * x + y
    targets = [(px, py, c) for px, py in _chip_peers(x, y)] + [(x, y, 1 - c)]
    return [(s, l.at[chip], dev, len(targets) * i + k) for i, (s, l) in enumerate(zip(srcs, lands)) for k, dev in enumerate(targets)]


def _sibling_copies(srcs, lands):
    x, y, c = _position()
    return [(_half_rows(srcs[0], 1 - c, srcs[0].shape[1]), lands[0], (x, y, 1 - c), 0)]


def _sibling_whole_copies(srcs, lands):
    x, y, c = _position()
    return [(srcs[0], lands[0], (x, y, 1 - c), 0)]


def _exchange_copies(srcs, lands):
    x, y, c = _position()
    return [(srcs[0].at[2 * px + py], lands[0].at[k], (px, py, c), k) for k, (px, py) in enumerate(_chip_peers(x, y))]


def _pair_sum_call(grad, recv, chip_core, name):
    _, h, B = recv.shape
    tr = _row_tile(h, B)

    def body(cc_ref, g_ref, r_ref, o_ref, ob_ref):
        s = g_ref[...] + r_ref[...]
        ob_ref[...] = s.astype(BF16)

        @pl.when(pl.program_id(1) == cc_ref[0])
        def _():
            o_ref[...] = s

    g_spec = pl.BlockSpec((None, tr, B), lambda i, q, cc_ref: (q, cc_ref[1] * (h // tr) + i, 0))
    spec = pl.BlockSpec((None, tr, B), lambda i, q, cc_ref: (q, i, 0))
    own_spec = pl.BlockSpec((tr, B), lambda i, q, cc_ref: (i, 0))
    return pl.pallas_call(
        body, name=name,
        grid_spec=pltpu.PrefetchScalarGridSpec(num_scalar_prefetch=1, grid=(h // tr, N_CHIPS), in_specs=[g_spec, spec],
                                               out_specs=[own_spec, spec]),
        out_shape=[jax.ShapeDtypeStruct((h, B), F32), jax.ShapeDtypeStruct(recv.shape, BF16)],
        compiler_params=_params(("parallel", "arbitrary")),
    )(chip_core, grad, recv)


def _chip_sum_call(partial, recv, chip_core, name):
    _, h, B = recv.shape
    tr = _row_tile(h, B)

    def body(cc_ref, p_ref, r_ref, o_ref):
        o_ref[...] = ((p_ref[...] + r_ref[0].astype(F32)) + r_ref[1].astype(F32)) + r_ref[2].astype(F32)

    return pl.pallas_call(
        body, name=name,
        grid_spec=pltpu.PrefetchScalarGridSpec(
            num_scalar_prefetch=1, grid=(h // tr,),
            in_specs=[pl.BlockSpec((tr, B), lambda i, cc_ref: (i, 0)),
                      pl.BlockSpec((3, tr, B), lambda i, cc_ref: (0, i, 0))],
            out_specs=pl.BlockSpec((tr, B), lambda i, cc_ref: (cc_ref[1] * (h // tr) + i, 0))),
        out_shape=jax.ShapeDtypeStruct((2 * h, B), F32),
        compiler_params=_params(("parallel",)),
    )(chip_core, partial, recv)


def _sibling_assemble_call(shards, name="grad_sibling_assemble"):
    n = len(shards)

    def body(*refs):
        ins, outs = refs[:n], refs[n:2 * n]
        send_sems, recv_sems = refs[2 * n:]
        x, y, c = _position()
        copies = []
        for i in range(n):
            rows = shards[i].shape[0]
            cp = pltpu.make_async_remote_copy(src_ref=_half_rows(ins[i], c, rows), dst_ref=_half_rows(outs[i], c, rows),
                                              send_sem=send_sems.at[i], recv_sem=recv_sems.at[i],
                                              device_id=(x, y, 1 - c), device_id_type=MESH)
            cp.start()
            copies.append(cp)
        for cp in copies:
            cp.wait()

    return pl.pallas_call(
        body, name=name,
        in_specs=[ANY] * n, out_specs=[ANY] * n,
        out_shape=[jax.ShapeDtypeStruct(s.shape, F32) for s in shards],
        input_output_aliases={i: i for i in range(n)},
        scratch_shapes=[pltpu.SemaphoreType.DMA((n,)), pltpu.SemaphoreType.DMA((n,))],
    )(*shards)


N_DEVICES = 8


def _allsum_copies(srcs, lands):
    x, y, c = _position()
    me = 4 * x + 2 * y + c
    out = []
    for k in range(1, N_DEVICES):
        peer = (1 - x if k & 4 else x, 1 - y if k & 2 else y, 1 - c if k & 1 else c)
        out.append((srcs[0], lands[0].at[me], peer, k - 1))
    return out


def _ordered_sum_call(mine, landed, me_chip, shapes, sharded_cols):
    rows = mine.shape[0]
    outs = [(s[0], n) if n else s for s, n in zip(shapes, sharded_cols)]

    def body(mc_ref, x_ref, l_ref, *refs):
        acc_ref = refs[-1]
        acc = jnp.where(mc_ref[0] == 0, x_ref[...], l_ref[0])
        for d in range(1, N_DEVICES):
            acc = acc + jnp.where(mc_ref[0] == d, x_ref[...], l_ref[d])
        acc_ref[...] = acc
        first = 0
        for o_ref, (r, c), n in zip(refs[:-1], shapes, sharded_cols):
            per_row = c // LANES

            def unpack(chip, o_ref=o_ref, r=r, n=n, per_row=per_row, first=first):
                for i in range(r):
                    for j in range((n or per_row * LANES) // LANES):
                        src = first + i * per_row + chip * ((n or 0) // LANES) + j
                        o_ref[i:i + 1, j * LANES:(j + 1) * LANES] = acc_ref[src:src + 1, :]

            if n:
                for q in range(N_CHIPS):
                    pl.when(mc_ref[1] == q)(functools.partial(unpack, q))
            else:
                unpack(0)
            first += r * per_row

    results = pl.pallas_call(
        body, name="small_grad_sum",
        in_specs=[pl.BlockSpec(memory_space=pltpu.SMEM), pl.BlockSpec(memory_space=pltpu.VMEM), pl.BlockSpec(memory_space=pltpu.VMEM)],
        out_specs=[pl.BlockSpec(memory_space=pltpu.VMEM)] * len(outs),
        out_shape=[jax.ShapeDtypeStruct(s, F32) for s in outs],
        scratch_shapes=[pltpu.VMEM((rows, LANES), F32)],
    )(me_chip, mine, landed)
    return results


def _pack(arrays):
    flat = jnp.concatenate([a.reshape(-1).astype(F32) for a in arrays])
    rows = -(-flat.shape[0] // LANES)
    rows = -(-rows // SUBLANES) * SUBLANES
    flat = jnp.pad(flat, (0, rows * LANES - flat.shape[0]))
    return flat.reshape(rows, LANES)


def _local_step(xs, target, P, late_weights, on_grad):
    S, D = xs.shape
    qkv_width = 3 * N_HEADS * HEAD_DIM
    glu_col0, gate_col0 = qkv_width, qkv_width + 2 * D
    shard_major = lambda g: g.reshape(N_CHIPS, g.shape[0] // N_CHIPS, g.shape[1])

    h1 = _rms_fwd_call(xs, P["norm_mix_pre"])
    buckets = _bucket_tables()
    bias = _bias_table_call(P["rel_bias"] + 0.0 * h1[0, 0].astype(F32), buckets)
    P = dict(P, **late_weights("in", bias))
    proj = _matmul(h1, P["w_in"], "nn", "proj_in")
    parts = []
    for g in range(N_GROUPS):
        parts += _attn_fwd_call(proj, bias, g)
    a, a_bf, lse = _attn_merge_call(parts)
    P = dict(P, **late_weights("mix", a_bf))
    y_a = _matmul(a_bf, P["w_attn_out"], "nn", "attn_out")
    c1 = _conv_fwd_call(proj, glu_col0, P["conv_dw_w"], P["conv_dw_b"])
    cact = _ln_silu_call(c1, P["conv_ln_g"], P["conv_ln_b"])
    y_c = _matmul(cact, P["conv_pw_w"], "nn", "conv_pw")
    mixed = _mix_call(proj, gate_col0, P["b_gate"], y_a, y_c)
    out = _matmul(mixed, P["w_out"], "nn", "mix_out")
    x1, h2 = _res1_call(xs, out, P["norm_mix_post"], P["norm_ffn_pre"])
    P = dict(P, **late_weights("up", h2))
    u = _matmul(h2, P["w_up"], "nn", "ffn_up")
    f = _ffn_fwd_call(u, P["ffn_conv_w"], P["ffn_conv_b"])
    P = dict(P, **late_weights("down", f))
    yff = _matmul(f, P["w_down"], "nn", "ffn_down")
    loss_tile, dx2, dyff, dg_ffn_post = _loss_call(yff, x1, P["norm_ffn_post"], target)

    G = {}
    G["norm_ffn_post"] = dg_ffn_post
    on_grad("w_down", shard_major(_matmul(f, dyff, "tn", "ffn_down_dw")))
    df = _matmul(dyff, P["w_down"], "nt", "ffn_down_dx")
    du, dwg, dwv, dbg, dbv = _ffn_bwd_call(u, P["ffn_conv_w"], P["ffn_conv_b"], df)
    G["ffn_conv_w"] = jnp.concatenate([dwg[:FFN_CONV_WIDTH], dwv[:FFN_CONV_WIDTH]], axis=1)
    G["ffn_conv_b"] = jnp.concatenate([dbg, dbv], axis=1)
    du = on_grad("w_up", functools.partial(_grad_half_matmul, h2, "ffn_up_dw"), carry=du)
    dh2 = _matmul(du, P["w_up"], "nt", "ffn_up_dx")
    dx1, dout, G["norm_ffn_pre"], G["norm_mix_post"] = _mid_bwd_call(x1, P["norm_ffn_pre"], dh2, dx2, out, P["norm_mix_post"])
    on_grad("w_out", shard_major(_matmul(mixed, dout, "tn", "mix_out_dw")))
    dmixed = _matmul(dout, P["w_out"], "nt", "mix_out_dx")
    dya, dyc, dga, dgc, dba, dbc = _mix_bwd_call(dmixed, proj, gate_col0, P["b_gate"], y_a, y_c)
    G["b_gate"] = jnp.concatenate([dba, dbc], axis=1)
    on_grad("w_attn_out", _matmul(a_bf, dya, "tn", "attn_out_dw", out_shards=True))
    dyc = on_grad("conv_pw_w", shard_major(_matmul(cact, dyc, "tn", "conv_pw_dw")), carry=dyc)
    da = _matmul(dya, P["w_attn_out"], "nt", "attn_out_dx")
    dcact = _matmul(dyc, P["conv_pw_w"], "nt", "conv_pw_dx")
    dc1, G["conv_ln_g"], G["conv_ln_b"] = _ln_silu_bwd_call(c1, P["conv_ln_g"], P["conv_ln_b"], dcact)
    dval, dgate, dw_dw, G["conv_dw_b"] = _conv_bwd_call(proj, glu_col0, P["conv_dw_w"], dc1)
    G["conv_dw_w"] = dw_dw[:CONV_WIDTH]
    delta = _attn_delta_call(a, da)
    dqs, dks, dvs, dbs = [], [], [], []
    for g in range(N_GROUPS):
        dq, dk, dv, db = _attn_bwd_call(proj, bias, da, lse, delta, g)
        dqs.append(dq)
        dks.append(dk)
        dvs.append(dv)
        dbs.append(db)
    G["rel_bias"] = _bias_grad_call(jnp.concatenate(dbs, axis=0), buckets)
    dproj = _dproj_call(dqs + dks + dvs, [dval, dgate, dga, dgc])
    dproj = on_grad("w_in", functools.partial(_grad_half_matmul, h1, "proj_in_dw"), carry=dproj)
    dh1 = _matmul(dproj, P["w_in"], "nt", "proj_in_dx")
    dh1 = on_grad(None, None, carry=dh1)
    grad_x, G["norm_mix_pre"] = _in_bwd_call(xs, P["norm_mix_pre"], dh1, dx1)
    return loss_tile, grad_x, G


def kernel(x, w_in, b_gate, rel_bias, w_attn_out, conv_dw_w, conv_dw_b, conv_ln_g, conv_ln_b, conv_pw_w, w_out, norm_mix_pre, norm_mix_post, norm_ffn_pre, norm_ffn_post, w_up, ffn_conv_w, ffn_conv_b, w_down, loss_target, m_w_in, m_b_gate, m_rel_bias, m_w_attn_out, m_conv_dw_w, m_conv_dw_b, m_conv_ln_g, m_conv_ln_b, m_conv_pw_w, m_w_out, m_norm_mix_pre, m_norm_mix_post, m_norm_ffn_pre, m_norm_ffn_post, m_w_up, m_ffn_conv_w, m_ffn_conv_b, m_w_down, v_w_in, v_b_gate, v_rel_bias, v_w_attn_out, v_conv_dw_w, v_conv_dw_b, v_conv_ln_g, v_conv_ln_b, v_conv_pw_w, v_w_out, v_norm_mix_pre, v_norm_mix_post, v_norm_ffn_pre, v_norm_ffn_post, v_w_up, v_ffn_conv_w, v_ffn_conv_b, v_w_down):
    weights = dict(w_in=w_in, b_gate=b_gate, rel_bias=rel_bias, w_attn_out=w_attn_out, conv_dw_w=conv_dw_w, conv_dw_b=conv_dw_b,
                   conv_ln_g=conv_ln_g, conv_ln_b=conv_ln_b, conv_pw_w=conv_pw_w, w_out=w_out, norm_mix_pre=norm_mix_pre,
                   norm_mix_post=norm_mix_post, norm_ffn_pre=norm_ffn_pre, norm_ffn_post=norm_ffn_post, w_up=w_up,
                   ffn_conv_w=ffn_conv_w, ffn_conv_b=ffn_conv_b, w_down=w_down)
    m_in = dict(w_in=m_w_in, b_gate=m_b_gate, rel_bias=m_rel_bias, w_attn_out=m_w_attn_out, conv_dw_w=m_conv_dw_w,
                conv_dw_b=m_conv_dw_b, conv_ln_g=m_conv_ln_g, conv_ln_b=m_conv_ln_b, conv_pw_w=m_conv_pw_w, w_out=m_w_out,
                norm_mix_pre=m_norm_mix_pre, norm_mix_post=m_norm_mix_post, norm_ffn_pre=m_norm_ffn_pre,
                norm_ffn_post=m_norm_ffn_post, w_up=m_w_up, ffn_conv_w=m_ffn_conv_w, ffn_conv_b=m_ffn_conv_b, w_down=m_w_down)
    v_in = dict(w_in=v_w_in, b_gate=v_b_gate, rel_bias=v_rel_bias, w_attn_out=v_w_attn_out, conv_dw_w=v_conv_dw_w,
                conv_dw_b=v_conv_dw_b, conv_ln_g=v_conv_ln_g, conv_ln_b=v_conv_ln_b, conv_pw_w=v_conv_pw_w, w_out=v_w_out,
                norm_mix_pre=v_norm_mix_pre, norm_mix_post=v_norm_mix_post, norm_ffn_pre=v_norm_ffn_pre,
                norm_ffn_post=v_norm_ffn_post, w_up=v_w_up, ffn_conv_w=v_ffn_conv_w, ffn_conv_b=v_ffn_conv_b, w_down=v_w_down)
    names = list(weights)
    xi, yi, ci = _position()
    chip = 2 * xi + yi
    core_arr = jnp.reshape(ci, (1,)).astype(jnp.int32)

    xs = x[0]
    target = loss_target[0]
    S, D = xs.shape

    big = ["w_in", "w_attn_out", "conv_pw_w", "w_out", "w_up", "w_down"]
    row_sharded = ("conv_pw_w", "w_out", "w_down")
    natural = lambda k, g: g.reshape(-1, g.shape[2]) if k in row_sharded else g
    first_srcs = [w_in[0].astype(BF16), conv_dw_w[0], ffn_conv_w[0]]
    first_lands = [lax.empty((N_CHIPS,) + s.shape, s.dtype) for s in first_srcs]
    (first_hop,) = _split_start("gather_in_start", [(first_srcs, first_lands, 4 * len(first_srcs), _first_hop_copies)], core_arr)
    launched = first_hop["token"]
    late_sets = dict(mix=["w_attn_out", "conv_pw_w", "w_out"], up=["w_up"], down=["w_down"])
    late_groups = []
    for keys in late_sets.values():
        srcs = [(weights[k][0] + launched).astype(BF16) for k in keys]
        late_groups.append((srcs, [lax.empty((N_CHIPS,) + s.shape, BF16) for s in srcs], 4 * len(keys), _gather_copies))
    started = {}

    def late_weights(tag, after):
        if tag == "in":
            w_in_halves, dw4, fc4 = _split_wait("gather_in_wait", first_hop, _first_hop_copies, after)[len(first_srcs):]
            second_hop, *late = _split_start("gather_in_pass_start", [([], [w_in_halves], 3, _second_hop_copies)] + late_groups, dw4)
            started.update(zip(late_sets, late))
            (w_in_full,) = _split_wait("gather_in_pass_wait", second_hop, _second_hop_copies, second_hop["tile"])
            return dict(w_in=w_in_full, conv_dw_w=jnp.concatenate(list(dw4), axis=1), ffn_conv_w=jnp.concatenate(list(fc4), axis=1))
        landed = _split_wait(f"gather_{tag}_wait", started[tag], _gather_copies, after)[len(late_sets[tag]):]
        return {k: natural(k, g) for k, g in zip(late_sets[tag], landed)}

    chip_core = jnp.stack([chip, ci]).astype(jnp.int32)
    exchanging, pending, second_half = {}, {}, {}

    held = []

    def launch(tag, after, carry=None):
        keys, groups, partial = [], [], {}
        for k in list(exchanging):
            st, copies = exchanging.pop(k)
            gk, r1 = _split_wait(f"sibling_exchange_wait_{k}", st, copies, after)
            if k in second_half:
                partial[k], s16 = second_half.pop(k)(init=r1)
            else:
                partial[k], s16 = _pair_sum_call(gk, r1, chip_core, f"pair_sum_{k}")
            keys.append(k)
            groups.append(([s16], [lax.empty((3,) + s16.shape[1:], BF16)], 3, _exchange_copies))
        fresh = [(k, copies) for k, _, copies in held]
        for _, g3, copies in held:
            rows = g3.shape[1] // 2 if copies is _sibling_copies else g3.shape[1]
            groups.append(([g3], [lax.empty((N_CHIPS, rows, g3.shape[2]), F32)], 1, copies))
        held.clear()
        begun = _split_start(f"grad_exchange_start_{tag}", groups, core_arr, carry)
        for k, st in zip(keys, begun):
            pending[k] = (partial[k], st)
        for (k, copies), st in zip(fresh, begun[len(keys):]):
            exchanging[k] = (st, copies)
        return begun[0]["carry"]

    def on_grad(k, g, carry=None):
        if k is None:
            return launch("last", carry[:SUBLANES, :LANES], carry)
        if callable(g):
            theirs = g(jnp.stack([chip, 1 - ci]).astype(jnp.int32), carry)
            held.append((k, theirs, _sibling_whole_copies))
            carried = launch(k, theirs[0, :SUBLANES, :LANES], carry)
            second_half[k] = functools.partial(g, chip_core, carried)
            return carried
        held.append((k, g, _sibling_copies))
        if k in ("w_down", "w_out", "w_attn_out"):
            return carry
        return launch(k, g[0, :SUBLANES, :LANES], carry)

    def finish(keys, after, tag):
        halves = []
        for k in keys:
            s32, st = pending[k]
            recv2 = _split_wait(f"chip_exchange_wait_{k}", st, _exchange_copies, after)[1]
            halves.append(_chip_sum_call(s32, recv2, chip_core, f"chip_sum_{k}"))
        return dict(zip(keys, _sibling_assemble_call(halves, f"grad_sibling_assemble_{tag}")))

    P = dict(b_gate=b_gate, rel_bias=rel_bias, conv_dw_b=conv_dw_b, conv_ln_g=conv_ln_g, conv_ln_b=conv_ln_b,
             norm_mix_pre=norm_mix_pre + launched, norm_mix_post=norm_mix_post, norm_ffn_pre=norm_ffn_pre,
             norm_ffn_post=norm_ffn_post, ffn_conv_b=ffn_conv_b)
    loss_tile, grad_x, G = _local_step(xs, target, P, late_weights, on_grad)

    small = [k for k in names if k not in big]
    packed = _pack([loss_tile[:1]] + [G[k] for k in small])
    (allsum,) = _split_start("small_grad_allsum_start",
                             [([packed], [jnp.zeros((N_DEVICES,) + packed.shape, F32)], N_DEVICES - 1, _allsum_copies)], core_arr)

    reduced, grads, deltas, new_m, new_v = {}, {}, {}, {}, {}

    def update(keys):
        for k in keys:
            gk, d, mn, vn = _adamw_call(weights[k][0], reduced[k], m_in[k][0], v_in[k][0], f"adamw_{k}")
            grads[k], deltas[k], new_m[k], new_v[k] = gk[None], d[None], mn[None], vn[None]

    others = [k for k in big if k != "w_in"]
    reduced.update(finish(others, allsum["tile"], "others"))
    update(others)
    reduced.update(finish(["w_in"], deltas["w_up"], "w_in"))
    update(["w_in"])

    me_chip = jnp.stack([4 * xi + 2 * yi + ci, chip]).astype(jnp.int32)
    mine, landed = _split_wait("small_grad_allsum_wait", allsum, _allsum_copies, deltas["w_in"])
    piece_shapes = [(1, LANES)] + [(G[k].size // LANES, LANES) if k == "rel_bias" else G[k].shape for k in small]
    piece_cols = [0] + [weights[k].shape[2] if k in ("conv_dw_w", "ffn_conv_w") else 0 for k in small]
    loss_row, *summed = _ordered_sum_call(mine, landed, me_chip, piece_shapes, piece_cols)
    loss = loss_row[0, 0]
    for k, gsum in zip(small, summed):
        grads[k] = gsum.reshape(weights[k].shape)
    ds, mns, vns = _adamw_small_call([weights[k] for k in small], [grads[k] for k in small],
                                     [m_in[k] for k in small], [v_in[k] for k in small])
    deltas.update(zip(small, ds))
    new_m.update(zip(small, mns))
    new_v.update(zip(small, vns))

    return (loss, grad_x[None], *[grads[k] for k in names], *[deltas[k] for k in names],
            *[new_m[k] for k in names], *[new_v[k] for k in names])
```

```python
import functools
import math

import jax
import jax.numpy as jnp
import numpy as np
from jax import lax
from jax.experimental import pallas as pl
from jax.experimental.pallas import tpu as pltpu

F32 = jnp.float32
BF16 = jnp.bfloat16
MESH = pl.DeviceIdType.MESH

HEAD_DIM = 128
HEADS_PER_GROUP = 4
DILATED_PATTERNS = ((128, 1), (512, 4), (2048, 16))
N_GROUPS = 3
N_HEADS = N_GROUPS * HEADS_PER_GROUP
SPAN = 128
GROUP_WIDTH = HEADS_PER_GROUP * HEAD_DIM
CONV_WIDTH = 31
FFN_CONV_WIDTH = 3
N_BUCKETS = 32
MAX_DISTANCE = 2048
RMS_EPS = 1e-6
LN_EPS = 1e-5
NEG_INF = -1e30
ADAM_LR = 0.001
ADAM_B1 = 0.9
ADAM_B2 = 0.999
ADAM_EPS = 1e-08
ADAM_WD = 0.01
ADAM_STEP = 10

LANES = 128
SUBLANES = 8
PACKED_ROWS = 16
ROW_TILE = 512
GATE_ROWS, GATE_COLS = 512, 512
TIME_BLOCK = 128
CONV_PAD = 32
FFN_PAD = 8
VMEM_LIMIT = 56 << 20


def _params(sem=None, vmem=None):
    kw = {}
    if sem is not None:
        kw["dimension_semantics"] = sem
    if vmem is not None:
        kw["vmem_limit_bytes"] = vmem
    return pltpu.CompilerParams(**kw)


def _pick(n, cands):
    for c in cands:
        if n % c == 0:
            return c
    return n


ELEMENTWISE_TILE_BYTES = 3 << 19


def _row_tile(rows, cols):
    for align in (16, SUBLANES):
        fits = [t for t in range(align, rows + 1, align) if rows % t == 0 and t * cols * 4 <= ELEMENTWISE_TILE_BYTES]
        if fits:
            return max(fits)
    return SUBLANES


N_CHIPS = 4
M_TILES = (1024, 1408, 512, 256, 128)
N_TILES = (1024, 512, 1408, 256, 128)
K_TILES = (2176, 2048, 1408, 1024, 512, 256, 128)


def _matmul(a, b, mode, name, out_shards=False, tm=None):
    assert a.dtype == BF16 and b.dtype == BF16, (name, a.dtype, b.dtype)
    b3 = b.ndim == 3
    tn = tk = None
    halves = None
    if mode == "nn":
        M, K = a.shape
        N = b.shape[-1] * (N_CHIPS if b3 else 1)
        tn = b.shape[-1] if b3 else None
    elif mode == "nt":
        if a.ndim == 3:
            halves = a.shape[2]
        M, K = a.shape[-2], a.shape[-1] * (a.shape[0] if a.ndim == 3 else 1)
        N = b.shape[-2]
        tk = b.shape[-1] if b3 else None
    else:
        if b3:
            halves = b.shape[2]
        K, M = a.shape
        N = b.shape[-1] * (b.shape[0] if b3 else 1)
        tn = N // N_CHIPS if out_shards else None
    tm = tm or _pick(M, M_TILES)
    tn = tn or _pick(N, N_TILES)
    tk = tk or _pick(K, K_TILES)
    nk = K // tk
    dn = {"nn": (((1,), (0,)), ((), ())), "nt": (((1,), (1,)), ((), ())), "tn": (((0,), (0,)), ((), ()))}[mode]

    def body(a_ref, b_ref, o_ref):
        if nk == 1:
            o_ref[...] = lax.dot_general(a_ref[...], b_ref[...], dn, preferred_element_type=F32)
        else:
            @pl.when(pl.program_id(2) == 0)
            def _():
                o_ref[...] = jnp.zeros_like(o_ref)

            o_ref[...] += lax.dot_general(a_ref[...], b_ref[...], dn, preferred_element_type=F32)

    if mode == "tn":
        a_spec = pl.BlockSpec((tk, tm), lambda i, j, k: (k, i))
    elif halves:
        per = halves // tk
        a_spec = pl.BlockSpec((None, tm, tk), lambda i, j, k: (k // per, i, k % per))
    else:
        a_spec = pl.BlockSpec((tm, tk), lambda i, j, k: (i, k))
    if mode == "nn":
        b_spec = pl.BlockSpec((None, tk, tn), lambda i, j, k: (j, k, 0)) if b3 else pl.BlockSpec((tk, tn), lambda i, j, k: (k, j))
    elif mode == "nt":
        b_spec = pl.BlockSpec((None, tn, tk), lambda i, j, k: (k, j, 0)) if b3 else pl.BlockSpec((tn, tk), lambda i, j, k: (j, k))
    elif halves:
        per = halves // tn
        b_spec = pl.BlockSpec((None, tk, tn), lambda i, j, k: (j // per, k, j % per))
    else:
        b_spec = pl.BlockSpec((tk, tn), lambda i, j, k: (k, j))
    if out_shards:
        out_spec = pl.BlockSpec((None, tm, tn), lambda i, j, k: (j, i, 0))
        out_shape = jax.ShapeDtypeStruct((N_CHIPS, M, tn), F32)
    else:
        out_spec = pl.BlockSpec((tm, tn), lambda i, j, k: (i, j))
        out_shape = jax.ShapeDtypeStruct((M, N), F32)
    return pl.pallas_call(
        body, name=name, grid=(M // tm, N // tn, nk),
        in_specs=[a_spec, b_spec], out_specs=out_spec, out_shape=out_shape,
        compiler_params=_params(("parallel", "parallel", "arbitrary"), VMEM_LIMIT),
    )(a, b)


def _grad_half_matmul(a, name, chip_half, b, init=None):
    K, M = a.shape
    parts = b.ndim == 3
    N = b.shape[-1] * (b.shape[0] if parts else 1)
    h, tn = M // 2, N // N_CHIPS
    summed = init is not None
    dn = (((0,), (0,)), ((), ()))

    def body(ch_ref, a_ref, b_ref, *rest):
        product = lax.dot_general(a_ref[...], b_ref[...], dn, preferred_element_type=F32)
        if not summed:
            rest[0][...] = product
            return
        init_ref, own_ref, sum16_ref = rest
        total = product + init_ref[...]
        sum16_ref[...] = total.astype(BF16)

        @pl.when(pl.program_id(0) == ch_ref[0])
        def _():
            own_ref[...] = total

    if parts:
        per = b.shape[2] // tn
        b_spec = pl.BlockSpec((None, K, tn), lambda j, ch_ref: (j // per, 0, j % per))
    else:
        b_spec = pl.BlockSpec((K, tn), lambda j, ch_ref: (0, j))
    shard_spec = pl.BlockSpec((None, h, tn), lambda j, ch_ref: (j, 0, 0))
    own_spec = pl.BlockSpec((h, tn), lambda j, ch_ref: (0, 0))
    shape = (N_CHIPS, h, tn)
    return pl.pallas_call(
        body, name=name + ("_mine" if summed else "_theirs"),
        grid_spec=pltpu.PrefetchScalarGridSpec(
            num_scalar_prefetch=1, grid=(N_CHIPS,),
            in_specs=[pl.BlockSpec((K, h), lambda j, ch_ref: (0, ch_ref[1])), b_spec] + [shard_spec] * summed,
            out_specs=[own_spec, shard_spec] if summed else shard_spec),
        out_shape=[jax.ShapeDtypeStruct((h, tn), F32), jax.ShapeDtypeStruct(shape, BF16)] if summed
        else jax.ShapeDtypeStruct(shape, F32),
        compiler_params=_params(("arbitrary",), VMEM_LIMIT),
    )(chip_half, a, b, *([init] if summed else []))


def _rms(x, g):
    r = lax.rsqrt(jnp.mean(x * x, axis=-1, keepdims=True) + RMS_EPS)
    return x * r * g


def _rms_bwd(x, g, dy):
    r = lax.rsqrt(jnp.mean(x * x, axis=-1, keepdims=True) + RMS_EPS)
    n = x * r
    dn = dy * g
    dx = r * (dn - n * jnp.mean(dn * n, axis=-1, keepdims=True))
    return dx, jnp.sum(dy * n, axis=0, keepdims=True)


def _sigmoid(x):
    return 1.0 / (1.0 + jnp.exp(-x))


_GELU_C = math.sqrt(2.0 / math.pi)


def _gelu(x):
    return 0.5 * x * (1.0 + jnp.tanh(_GELU_C * (x + 0.044715 * x * x * x)))


def _gelu_and_grad(x):
    x2 = x * x
    t = jnp.tanh(_GELU_C * x * (1.0 + 0.044715 * x2))
    half = 0.5 * (1.0 + t)
    return x * half, half + (0.5 * _GELU_C) * x * (1.0 - t * t) * (1.0 + (3.0 * 0.044715) * x2)


def _row_spec(width, col_block=0):
    return pl.BlockSpec((ROW_TILE, width), lambda i: (i, col_block))


def _vec_spec(width, col_block=0):
    return pl.BlockSpec((1, width), lambda i: (0, col_block))


def _accumulate(ref, part):
    @pl.when(pl.program_id(0) == 0)
    def _():
        ref[...] = part

    @pl.when(pl.program_id(0) > 0)
    def _():
        ref[...] += part


def _rms_fwd_call(x, g):
    S, D = x.shape

    def body(x_ref, g_ref, h_ref):
        h_ref[...] = _rms(x_ref[...], g_ref[...]).astype(BF16)

    return pl.pallas_call(
        body, name="rms_mix_pre", grid=(S // ROW_TILE,),
        in_specs=[_row_spec(D), _vec_spec(D)], out_specs=_row_spec(D),
        out_shape=jax.ShapeDtypeStruct((S, D), BF16),
        compiler_params=_params(("parallel",)),
    )(x, g)


def _ln_silu_call(c1, g, b):
    S, C = c1.shape

    def body(c_ref, g_ref, b_ref, o_ref):
        xv = c_ref[...]
        mu = jnp.mean(xv, axis=-1, keepdims=True)
        xc = xv - mu
        var = jnp.mean(xc * xc, axis=-1, keepdims=True)
        z = xc * lax.rsqrt(var + LN_EPS) * g_ref[...] + b_ref[...]
        o_ref[...] = (z * _sigmoid(z)).astype(BF16)

    return pl.pallas_call(
        body, name="conv_ln_silu", grid=(S // ROW_TILE,),
        in_specs=[_row_spec(C), _vec_spec(C), _vec_spec(C)], out_specs=_row_spec(C),
        out_shape=jax.ShapeDtypeStruct((S, C), BF16),
        compiler_params=_params(("parallel",)),
    )(c1, g, b)


def _ln_silu_bwd_call(c1, g, b, dc):
    S, C = c1.shape

    def body(c_ref, g_ref, b_ref, dc_ref, dx_ref, dg_ref, db_ref):
        xv = c_ref[...]
        mu = jnp.mean(xv, axis=-1, keepdims=True)
        xc = xv - mu
        rs = lax.rsqrt(jnp.mean(xc * xc, axis=-1, keepdims=True) + LN_EPS)
        xh = xc * rs
        z = xh * g_ref[...] + b_ref[...]
        sg = _sigmoid(z)
        dz = dc_ref[...] * (sg * (1.0 + z * (1.0 - sg)))
        dxh = dz * g_ref[...]
        dx_ref[...] = rs * (dxh - jnp.mean(dxh, axis=-1, keepdims=True) - xh * jnp.mean(dxh * xh, axis=-1, keepdims=True))
        _accumulate(dg_ref, jnp.sum(dz * xh, axis=0, keepdims=True))
        _accumulate(db_ref, jnp.sum(dz, axis=0, keepdims=True))

    return pl.pallas_call(
        body, name="conv_ln_silu_bwd", grid=(S // ROW_TILE,),
        in_specs=[_row_spec(C), _vec_spec(C), _vec_spec(C), _row_spec(C)],
        out_specs=[_row_spec(C), _vec_spec(C), _vec_spec(C)],
        out_shape=[jax.ShapeDtypeStruct((S, C), F32), jax.ShapeDtypeStruct((1, C), F32), jax.ShapeDtypeStruct((1, C), F32)],
        compiler_params=_params(("arbitrary",)),
    )(c1, g, b, dc)


def _mix_call(proj, gate_col0, b_gate, y_a, y_c):
    S, D = y_a.shape
    w = GATE_COLS
    nc = D // w
    ga0, gc0 = gate_col0 // w, (gate_col0 + D) // w

    def body(ga_ref, gc_ref, ba_ref, bc_ref, ya_ref, yc_ref, o_ref):
        o_ref[...] = (_sigmoid(ga_ref[...] + ba_ref[...]) * ya_ref[...]
                      + _sigmoid(gc_ref[...] + bc_ref[...]) * yc_ref[...]).astype(BF16)

    tile = lambda off: pl.BlockSpec((GATE_ROWS, w), lambda i, j: (i, off + j))
    vec = lambda off: pl.BlockSpec((1, w), lambda i, j: (0, off + j))
    return pl.pallas_call(
        body, name="gate_mix", grid=(S // GATE_ROWS, nc),
        in_specs=[tile(ga0), tile(gc0), vec(0), vec(nc), tile(0), tile(0)],
        out_specs=tile(0), out_shape=jax.ShapeDtypeStruct((S, D), BF16),
        compiler_params=_params(("parallel", "parallel")),
    )(proj, proj, b_gate, b_gate, y_a, y_c)


def _window_stores(stage_ref, slot, dst_ref, rows, cols, sems):
    width = stage_ref.shape[-1]
    return [pltpu.make_async_copy(stage_ref.at[slot, p], dst_ref.at[rows, pl.ds(pl.multiple_of(c, LANES), width)], sems.at[slot, p])
            for p, c in enumerate(cols)]


def _staged_window_stores(stage_ref, dst_ref, sems, step, n_steps, rows, cols, fill):
    slot = step % 2
    copies = lambda s: _window_stores(stage_ref, s, dst_ref, rows, cols, sems)

    @pl.when(step >= 2)
    def _():
        for cp in copies(slot):
            cp.wait()

    fill(slot)
    for cp in copies(slot):
        cp.start()

    @pl.when(step == n_steps - 1)
    def _():
        for s in ([slot, 1 - slot] if n_steps > 1 else [slot]):
            for cp in copies(s):
                cp.wait()


def _mix_bwd_call(dmixed, proj, gate_col0, b_gate, y_a, y_c):
    S, D = y_a.shape
    w = GATE_COLS
    nc = D // w
    nr = S // GATE_ROWS
    ga0, gc0 = gate_col0 // w, (gate_col0 + D) // w

    def body(dm_ref, ga_ref, gc_ref, ba_ref, bc_ref, ya_ref, yc_ref, dya_ref, dyc_ref, dproj_ref, dba_ref, dbc_ref,
             stage_ref, sems):
        j, i = pl.program_id(0), pl.program_id(1)
        dm = dm_ref[...]
        sa = _sigmoid(ga_ref[...] + ba_ref[...])
        sc = _sigmoid(gc_ref[...] + bc_ref[...])
        dya_ref[...] = (dm * sa).astype(BF16)
        dyc_ref[...] = (dm * sc).astype(BF16)
        dga = dm * ya_ref[...] * sa * (1.0 - sa)
        dgc = dm * yc_ref[...] * sc * (1.0 - sc)

        def fill(slot):
            stage_ref[slot, 0] = dga.astype(BF16)
            stage_ref[slot, 1] = dgc.astype(BF16)

        rows = pl.ds(pl.multiple_of(i * GATE_ROWS, GATE_ROWS), GATE_ROWS)
        _staged_window_stores(stage_ref, dproj_ref, sems, j * nr + i, nc * nr, rows,
                              [gate_col0 + j * w, gate_col0 + D + j * w], fill)
        pa = jnp.sum(dga, axis=0, keepdims=True)
        pc = jnp.sum(dgc, axis=0, keepdims=True)

        @pl.when(i == 0)
        def _():
            dba_ref[...] = pa
            dbc_ref[...] = pc

        @pl.when(i > 0)
        def _():
            dba_ref[...] += pa
            dbc_ref[...] += pc

    tile = lambda off: pl.BlockSpec((GATE_ROWS, w), lambda j, i: (i, off + j))
    vec = lambda off: pl.BlockSpec((1, w), lambda j, i: (0, off + j))
    return pl.pallas_call(
        body, name="gate_mix_bwd", grid=(nc, nr),
        in_specs=[tile(0), tile(ga0), tile(gc0), vec(0), vec(nc), tile(0), tile(0)],
        out_specs=[tile(0), tile(0), ANY, vec(0), vec(0)],
        out_shape=[jax.ShapeDtypeStruct((S, D), BF16)] * 2 + [jax.ShapeDtypeStruct((S, proj.shape[1]), BF16)] + [
                   jax.ShapeDtypeStruct((1, D), F32), jax.ShapeDtypeStruct((1, D), F32)],
        scratch_shapes=[pltpu.VMEM((2, 2, GATE_ROWS, w), BF16), pltpu.SemaphoreType.DMA((2, 2))],
        compiler_params=_params(("arbitrary", "arbitrary")),
    )(dmixed, proj, proj, b_gate, b_gate, y_a, y_c)


def _res1_call(x, out, g_post, g_pre):
    S, D = x.shape

    def body(x_ref, o_ref, gp_ref, gq_ref, x1_ref, h2_ref):
        x1 = x_ref[...] + _rms(o_ref[...], gp_ref[...])
        x1_ref[...] = x1
        h2_ref[...] = _rms(x1, gq_ref[...]).astype(BF16)

    return pl.pallas_call(
        body, name="residual_mix", grid=(S // ROW_TILE,),
        in_specs=[_row_spec(D), _row_spec(D), _vec_spec(D), _vec_spec(D)],
        out_specs=[_row_spec(D), _row_spec(D)],
        out_shape=[jax.ShapeDtypeStruct((S, D), F32), jax.ShapeDtypeStruct((S, D), BF16)],
        compiler_params=_params(("parallel",)),
    )(x, out, g_post, g_pre)


def _loss_call(y, x1, g_post, target):
    S, D = y.shape

    def body(y_ref, x1_ref, g_ref, t_ref, loss_ref, dx_ref, dy_ref, dg_ref):
        yv, gv = y_ref[...], g_ref[...]
        err = x1_ref[...] + _rms(yv, gv) - t_ref[...]
        dx2 = err * (1.0 / D)
        dx_ref[...] = dx2
        dy, dg = _rms_bwd(yv, gv, dx2)
        dy_ref[...] = dy.astype(BF16)
        _accumulate(dg_ref, dg)
        part = 0.5 * jnp.sum(jnp.mean(err * err, axis=-1, keepdims=True), axis=0, keepdims=True)
        _accumulate(loss_ref, jnp.broadcast_to(part, (SUBLANES, LANES)))

    return pl.pallas_call(
        body, name="residual_ffn_loss", grid=(S // ROW_TILE,),
        in_specs=[_row_spec(D), _row_spec(D), _vec_spec(D), _row_spec(D)],
        out_specs=[pl.BlockSpec((SUBLANES, LANES), lambda i: (0, 0)), _row_spec(D), _row_spec(D), _vec_spec(D)],
        out_shape=[jax.ShapeDtypeStruct((SUBLANES, LANES), F32), jax.ShapeDtypeStruct((S, D), F32),
                   jax.ShapeDtypeStruct((S, D), BF16), jax.ShapeDtypeStruct((1, D), F32)],
        compiler_params=_params(("arbitrary",)),
    )(y, x1, g_post, target)


def _mid_bwd_call(x1, g_pre, dh2, dx2, out, g_post):
    S, D = x1.shape

    def body(x1_ref, gq_ref, dh_ref, dx2_ref, o_ref, gp_ref, dx1_ref, do_ref, dgq_ref, dgp_ref):
        d, dgq = _rms_bwd(x1_ref[...], gq_ref[...], dh_ref[...])
        dx1 = dx2_ref[...] + d
        dx1_ref[...] = dx1
        do, dgp = _rms_bwd(o_ref[...], gp_ref[...], dx1)
        do_ref[...] = do.astype(BF16)
        _accumulate(dgq_ref, dgq)
        _accumulate(dgp_ref, dgp)

    return pl.pallas_call(
        body, name="residual_mix_bwd", grid=(S // ROW_TILE,),
        in_specs=[_row_spec(D), _vec_spec(D), _row_spec(D), _row_spec(D), _row_spec(D), _vec_spec(D)],
        out_specs=[_row_spec(D), _row_spec(D), _vec_spec(D), _vec_spec(D)],
        out_shape=[jax.ShapeDtypeStruct((S, D), F32), jax.ShapeDtypeStruct((S, D), BF16)] + [jax.ShapeDtypeStruct((1, D), F32)] * 2,
        compiler_params=_params(("arbitrary",)),
    )(x1, g_pre, dh2, dx2, out, g_post)


def _in_bwd_call(x, g, dh1, dx1):
    S, D = x.shape

    def body(x_ref, g_ref, dh_ref, dx1_ref, gx_ref, dg_ref):
        d, dg = _rms_bwd(x_ref[...], g_ref[...], dh_ref[...])
        gx_ref[...] = dx1_ref[...] + d
        _accumulate(dg_ref, dg)

    return pl.pallas_call(
        body, name="rms_mix_pre_bwd", grid=(S // ROW_TILE,),
        in_specs=[_row_spec(D), _vec_spec(D), _row_spec(D), _row_spec(D)],
        out_specs=[_row_spec(D), _vec_spec(D)],
        out_shape=[jax.ShapeDtypeStruct((S, D), F32), jax.ShapeDtypeStruct((1, D), F32)],
        compiler_params=_params(("arbitrary",)),
    )(x, g, dh1, dx1)


def _bucket_table(dilation):
    qi = np.arange(SPAN)[:, None]
    ki = np.arange(2 * SPAN)[None, :]
    dist = np.maximum(qi + SPAN - ki, 0) * dilation
    max_exact = N_BUCKETS // 2
    d = np.maximum(dist, 1).astype(np.float64)
    large = max_exact + (np.log(d / max_exact) / math.log(MAX_DISTANCE / max_exact) * (N_BUCKETS - max_exact)).astype(np.int32)
    large = np.minimum(large, N_BUCKETS - 1)
    return np.where(dist < max_exact, dist, large).astype(np.int32)


def _bucket_tables():
    return jnp.asarray(np.stack([_bucket_table(r) for _, r in DILATED_PATTERNS]))


def _bias_table_call(rel_bias, buckets):
    def body(rb_ref, bk_ref, o_ref):
        for h in range(N_HEADS):
            bk = bk_ref[h // HEADS_PER_GROUP]

            def step(b, acc):
                return jnp.where(bk == b, rb_ref[b, h], acc)

            o_ref[h] = lax.fori_loop(0, N_BUCKETS, step, jnp.zeros((SPAN, 2 * SPAN), F32))

    return pl.pallas_call(
        body, name="rel_bias_table",
        in_specs=[pl.BlockSpec(memory_space=pltpu.SMEM), pl.BlockSpec(memory_space=pltpu.VMEM)],
        out_specs=pl.BlockSpec(memory_space=pltpu.VMEM),
        out_shape=jax.ShapeDtypeStruct((N_HEADS, SPAN, 2 * SPAN), F32),
    )(rel_bias, buckets)


def _bias_grad_call(dbias, buckets):
    def body(db_ref, bk_ref, o_ref, rows_ref):
        for h in range(N_HEADS):
            bk = bk_ref[h // HEADS_PER_GROUP]
            dv = db_ref[h]

            def step(b, carry):
                rows_ref[h, b] = jnp.sum(jnp.where(bk == b, dv, 0.0), axis=0, keepdims=True)
                return carry

            lax.fori_loop(0, N_BUCKETS, step, 0)
        o_ref[...] = jnp.sum(rows_ref[...], axis=-1, keepdims=True)

    out = pl.pallas_call(
        body, name="rel_bias_grad",
        in_specs=[pl.BlockSpec(memory_space=pltpu.VMEM), pl.BlockSpec(memory_space=pltpu.VMEM)],
        out_specs=pl.BlockSpec(memory_space=pltpu.VMEM),
        out_shape=jax.ShapeDtypeStruct((N_HEADS, N_BUCKETS, 1, 1), F32),
        scratch_shapes=[pltpu.VMEM((N_HEADS, N_BUCKETS, 1, 2 * SPAN), F32)],
    )(dbias, buckets)
    return out.reshape(N_HEADS, N_BUCKETS).T


def _dot_nt(a, b):
    return lax.dot_general(a, b, (((1,), (1,)), ((), ())), preferred_element_type=F32)


def _dot_nn(a, b):
    return lax.dot_general(a, b, (((1,), (0,)), ((), ())), preferred_element_type=F32)


def _dot_tn(a, b):
    return lax.dot_general(a, b, (((0,), (0,)), ((), ())), preferred_element_type=F32)


def _band_masks(n, nb):
    qi = lax.broadcasted_iota(jnp.int32, (SPAN, SPAN), 0)
    ki = lax.broadcasted_iota(jnp.int32, (SPAN, SPAN), 1)
    prev_ok = jnp.logical_and(ki >= qi, n > 0)
    cur_ok = ki <= qi
    next_ok = jnp.logical_and(ki >= qi, n < nb - 1)
    return prev_ok, cur_ok, next_ok


def _wide_band_mask(n):
    qi = lax.broadcasted_iota(jnp.int32, (SPAN, 2 * SPAN), 0)
    ki = lax.broadcasted_iota(jnp.int32, (SPAN, 2 * SPAN), 1)
    prev_ok = jnp.logical_and(jnp.logical_and(ki < SPAN, ki >= qi), n > 0)
    cur_ok = jnp.logical_and(ki >= SPAN, ki - SPAN <= qi)
    return jnp.logical_or(prev_ok, cur_ok)


def _attn_plan(S, group):
    r = DILATED_PATTERNS[group][1]
    hp, per = (HEADS_PER_GROUP, 1) if r == 1 else (2, 4)
    return r, S // (r * SPAN), hp, per


def _residue_rows(rho, r):
    return slice(None) if r == 1 else pl.ds(rho, SPAN, stride=r)


def _for_residues(r, per, fn):
    if r == per:
        for u in range(per):
            fn(u)
        return

    def step(i, carry):
        for u in range(per):
            fn(i * per + u)
        return carry

    lax.fori_loop(0, r // per, step, 0)


def _attn_fwd_call(proj, bias, group):
    S = proj.shape[0]
    r, nb, hp, per = _attn_plan(S, group)
    scale = HEAD_DIM ** -0.5
    kinds = ("q", "kp", "kc", "vp", "vc") if nb > 1 else ("q", "kc", "vc")

    per_kind = _refs_per_kind(r, hp)

    def body(*refs):
        ins = {kind: refs[i * per_kind:(i + 1) * per_kind] for i, kind in enumerate(kinds)}
        b_ref, o_ref, lse_ref = refs[len(kinds) * per_kind:]
        n = pl.program_id(1)
        prev_ok, cur_ok, _ = _band_masks(n, nb)

        band_ok = _wide_band_mask(n) if nb > 1 else cur_ok

        def residue(rho):
            rows = _residue_rows(rho, r)
            for j in range(hp):
                get = lambda kind: _head_rows(ins[kind], j, rows, r).astype(BF16)
                q = get("q")
                if nb > 1:
                    keys, vals, bias_j = jnp.concatenate([get("kp"), get("kc")], axis=0), jnp.concatenate([get("vp"), get("vc")], axis=0), b_ref[j]
                else:
                    keys, vals, bias_j = get("kc"), get("vc"), b_ref[j, :, SPAN:]
                s = jnp.where(band_ok, _dot_nt(q, keys) * scale + bias_j, NEG_INF)
                m = jnp.max(s, axis=-1, keepdims=True)
                p = jnp.exp(s - m)
                den = jnp.sum(p, axis=-1, keepdims=True)
                o_ref[j, rows, :] = _dot_nn(p.astype(BF16), vals) / den
                lse_ref[j, rows, :] = jnp.broadcast_to(m + jnp.log(den), (SPAN, HEAD_DIM))

        _for_residues(r, per, residue)

    in_specs = [_head_spec(r, nb, hp, kind, group, jj) for kind in kinds for jj in range(per_kind)]
    in_specs.append(pl.BlockSpec((hp, SPAN, 2 * SPAN), lambda j, n: (group * (HEADS_PER_GROUP // hp) + j, 0, 0)))
    out = pl.BlockSpec((hp, r * SPAN, HEAD_DIM), lambda j, n: (j, n, 0))
    return pl.pallas_call(
        body, name=f"attn_fwd_g{group}", grid=(HEADS_PER_GROUP // hp, nb),
        in_specs=in_specs, out_specs=[out] * 2,
        out_shape=[jax.ShapeDtypeStruct((HEADS_PER_GROUP, S, HEAD_DIM), F32)] * 2,
        compiler_params=_params(("parallel", "parallel"), VMEM_LIMIT),
    )(*([proj] * (len(in_specs) - 1)), bias)


_PROJ_PART = dict(q=0, qn=0, kp=1, kc=1, vp=2, vc=2)


def _refs_per_kind(r, hp):
    return 1 if r == 1 else hp


def _head_rows(refs, j, rows, r):
    return refs[0][:, j * HEAD_DIM:(j + 1) * HEAD_DIM] if r == 1 else refs[j][rows, :]


def _head_spec(r, nb, hp, kind, group, jj):
    if kind in _PROJ_PART:
        base = (_PROJ_PART[kind] * N_GROUPS + group) * HEADS_PER_GROUP
    else:
        base = 0
    if kind.endswith("p"):
        row = lambda n: jnp.maximum(n - 1, 0)
    elif kind.endswith("n"):
        row = lambda n: jnp.minimum(n + 1, nb - 1)
    else:
        row = lambda n: n
    if r == 1:
        return pl.BlockSpec((SPAN, hp * HEAD_DIM), lambda j, n: (row(n), base // hp + j))
    return pl.BlockSpec((r * SPAN, HEAD_DIM), lambda j, n: (row(n), base + j * hp + jj))


def _attn_merge_call(parts):
    S = parts[0].shape[1]

    def body(o1, s1, o2, s2, o3, s3, a_ref, ab_ref, lse_ref):
        for j in range(HEADS_PER_GROUP):
            sl = slice(j * HEAD_DIM, (j + 1) * HEAD_DIM)
            mx = jnp.maximum(jnp.maximum(s1[j], s2[j]), s3[j])
            w1 = jnp.exp(s1[j] - mx)
            w2 = jnp.exp(s2[j] - mx)
            w3 = jnp.exp(s3[j] - mx)
            den = w1 + w2 + w3
            a = (w1 * o1[j] + w2 * o2[j] + w3 * o3[j]) / den
            a_ref[:, sl] = a
            ab_ref[:, sl] = a.astype(BF16)
            lse_ref[:, sl] = mx + jnp.log(den)

    heads = pl.BlockSpec((HEADS_PER_GROUP, ROW_TILE, HEAD_DIM), lambda i: (0, i, 0))
    return pl.pallas_call(
        body, name="attn_merge", grid=(S // ROW_TILE,),
        in_specs=[heads] * 6, out_specs=[_row_spec(GROUP_WIDTH)] * 3,
        out_shape=[jax.ShapeDtypeStruct((S, GROUP_WIDTH), F32), jax.ShapeDtypeStruct((S, GROUP_WIDTH), BF16),
                   jax.ShapeDtypeStruct((S, GROUP_WIDTH), F32)],
        compiler_params=_params(("parallel",)),
    )(*parts)


def _attn_delta_call(a, da):
    S = a.shape[0]

    def body(a_ref, da_ref, d_ref):
        for j in range(HEADS_PER_GROUP):
            sl = slice(j * HEAD_DIM, (j + 1) * HEAD_DIM)
            d = jnp.sum(a_ref[:, sl] * da_ref[:, sl], axis=-1, keepdims=True)
            d_ref[:, sl] = jnp.broadcast_to(d, (ROW_TILE, HEAD_DIM))

    return pl.pallas_call(
        body, name="attn_delta", grid=(S // ROW_TILE,),
        in_specs=[_row_spec(GROUP_WIDTH)] * 2, out_specs=_row_spec(GROUP_WIDTH),
        out_shape=jax.ShapeDtypeStruct((S, GROUP_WIDTH), F32),
        compiler_params=_params(("parallel",)),
    )(a, da)


def _attn_bwd_call(proj, bias, da, lse, delta, group):
    S = proj.shape[0]
    r, nb, hp, per = _attn_plan(S, group)
    scale = HEAD_DIM ** -0.5
    kinds = ("q", "qn", "kp", "kc", "vp", "vc", "da", "dan", "lse", "lsen", "dl", "dln") if nb > 1 else ("q", "kc", "vc", "da", "lse", "dl")
    source = dict(da=da, dan=da, lse=lse, lsen=lse, dl=delta, dln=delta)

    per_kind = _refs_per_kind(r, hp)

    def body(*refs):
        ins = {kind: refs[i * per_kind:(i + 1) * per_kind] for i, kind in enumerate(kinds)}
        b_ref, dq_ref, dk_ref, dv_ref, db_ref = refs[len(kinds) * per_kind:]
        n = pl.program_id(1)
        prev_ok, cur_ok, next_ok = _band_masks(n, nb)

        @pl.when(n == 0)
        def _():
            db_ref[...] = jnp.zeros_like(db_ref)

        band_ok = _wide_band_mask(n) if nb > 1 else cur_ok

        def residue(rho):
            rows = _residue_rows(rho, r)
            for j in range(hp):
                get = lambda kind: _head_rows(ins[kind], j, rows, r)
                q = get("q").astype(BF16)
                kc = get("kc").astype(BF16)
                vc = get("vc").astype(BF16)
                dav = get("da").astype(BF16)
                lse_q, dl_q = get("lse"), get("dl")
                if nb == 1:
                    pc = jnp.exp(jnp.where(cur_ok, _dot_nt(q, kc) * scale + b_ref[j, :, SPAN:], NEG_INF) - lse_q)
                    dsc = pc * (_dot_nt(dav, vc) - dl_q)
                    dsc_b = dsc.astype(BF16)
                    dq = _dot_nn(dsc_b, kc)
                    dk = _dot_tn(dsc_b, q)
                    dv = _dot_tn(pc.astype(BF16), dav)
                    db_ref[j, :, SPAN:] += dsc
                else:
                    qn = get("qn").astype(BF16)
                    dan = get("dan").astype(BF16)
                    keys = jnp.concatenate([get("kp").astype(BF16), kc], axis=0)
                    vals = jnp.concatenate([get("vp").astype(BF16), vc], axis=0)
                    wide = lambda t: jnp.concatenate([t, t], axis=1)
                    p = jnp.exp(jnp.where(band_ok, _dot_nt(q, keys) * scale + b_ref[j], NEG_INF) - wide(lse_q))
                    ds = p * (_dot_nt(dav, vals) - wide(dl_q))
                    dq = _dot_nn(ds.astype(BF16), keys)
                    db_ref[j] += ds
                    pn = jnp.exp(jnp.where(next_ok, _dot_nt(qn, kc) * scale + b_ref[j, :, :SPAN], NEG_INF) - get("lsen"))
                    dsn = pn * (_dot_nt(dan, vc) - get("dln"))
                    both = lambda cur_part, next_part: jnp.concatenate([cur_part.astype(BF16), next_part.astype(BF16)], axis=0)
                    dk = _dot_tn(both(ds[:, SPAN:], dsn), jnp.concatenate([q, qn], axis=0))
                    dv = _dot_tn(both(p[:, SPAN:], pn), jnp.concatenate([dav, dan], axis=0))
                dq_ref[j, rows, :] = dq * scale
                dk_ref[j, rows, :] = dk * scale
                dv_ref[j, rows, :] = dv

        _for_residues(r, per, residue)

    per_group = HEADS_PER_GROUP // hp
    band = (hp, SPAN, 2 * SPAN)
    in_specs = [_head_spec(r, nb, hp, kind, group, jj) for kind in kinds for jj in range(per_kind)]
    in_specs.append(pl.BlockSpec(band, lambda j, n: (group * per_group + j, 0, 0)))
    operands = [source.get(kind, proj) for kind in kinds for _ in range(per_kind)] + [bias]
    out = pl.BlockSpec((hp, r * SPAN, HEAD_DIM), lambda j, n: (j, n, 0))
    return pl.pallas_call(
        body, name=f"attn_bwd_g{group}", grid=(per_group, nb),
        in_specs=in_specs,
        out_specs=[out] * 3 + [pl.BlockSpec(band, lambda j, n: (j, 0, 0))],
        out_shape=[jax.ShapeDtypeStruct((HEADS_PER_GROUP, S, HEAD_DIM), F32)] * 3
        + [jax.ShapeDtypeStruct((HEADS_PER_GROUP, SPAN, 2 * SPAN), F32)],
        compiler_params=_params(("parallel", "arbitrary"), VMEM_LIMIT),
    )(*operands)


def _dproj_call(dqkv, dproj):
    S = dproj.shape[0]
    n = len(dqkv)

    def body(*refs):
        o_ref = refs[-1]
        col = 0
        for ref in refs[:n]:
            for j in range(HEADS_PER_GROUP):
                o_ref[:, col:col + HEAD_DIM] = ref[j].astype(BF16)
                col += HEAD_DIM

    heads = pl.BlockSpec((HEADS_PER_GROUP, ROW_TILE, HEAD_DIM), lambda i: (0, i, 0))
    return pl.pallas_call(
        body, name="dproj_assemble", grid=(S // ROW_TILE,),
        in_specs=[heads] * n + [ANY],
        out_specs=_row_spec(n * GROUP_WIDTH), out_shape=jax.ShapeDtypeStruct(dproj.shape, BF16),
        input_output_aliases={n: 0},
        compiler_params=_params(("parallel",)),
    )(*dqkv, dproj)


def _tap_rows(xpad_ref, t0, k, width, pad):
    return xpad_ref[pl.ds(t0 + (pad - (width - 1 - k)), TIME_BLOCK), :]


def _conv_block(xpad_ref, t0, w_ref, width, pad):
    acc = None
    for k in range(width):
        term = w_ref[k:k + 1, :] * _tap_rows(xpad_ref, t0, k, width, pad)
        acc = term if acc is None else acc + term
    return acc


def _conv_transpose_block(dpad_ref, t0, w_ref, width):
    acc = None
    for k in range(width):
        term = w_ref[k:k + 1, :] * dpad_ref[pl.ds(t0 + (width - 1 - k), TIME_BLOCK), :]
        acc = term if acc is None else acc + term
    return acc


def _conv_weight_grad(xpad_ref, t0, dy, dw_ref, width, pad):
    for k in range(width):
        dw_ref[k:k + 1, :] += jnp.sum(dy * _tap_rows(xpad_ref, t0, k, width, pad), axis=0, keepdims=True)


def _time_loop(S, step, skip_first=0, skip_last=0):
    def it(tb, carry):
        step(pl.multiple_of(tb * TIME_BLOCK, TIME_BLOCK))
        return carry

    lax.fori_loop(skip_first, S // TIME_BLOCK - skip_last, it, 0)


def _fill_head(head_ref, x_ref, pad):
    head_ref[0:pad, :] = jnp.zeros((pad, LANES), F32)
    head_ref[pad:, :] = x_ref[0:TIME_BLOCK, :]


def _fill_tail(tail_ref, x_ref, pad):
    S = x_ref.shape[0]
    tail_ref[0:TIME_BLOCK, :] = x_ref[S - TIME_BLOCK:S, :]
    tail_ref[TIME_BLOCK:, :] = jnp.zeros((pad, LANES), F32)


def _conv_fwd_call(proj, col0, w, b):
    S = proj.shape[0]
    C = w.shape[1]
    nt = C // LANES
    v0, g0 = col0 // LANES, (col0 + C) // LANES

    def body(val_ref, gate_ref, w_ref, b_ref, o_ref, pad_ref):
        pad_ref[0:CONV_PAD, :] = jnp.zeros((CONV_PAD, LANES), F32)
        pad_ref[CONV_PAD:, :] = val_ref[...] * _sigmoid(gate_ref[...])

        def step(t0):
            o_ref[pl.ds(t0, TIME_BLOCK), :] = _conv_block(pad_ref, t0, w_ref, CONV_WIDTH, CONV_PAD) + b_ref[...]

        _time_loop(S, step)

    seq = lambda off: pl.BlockSpec((S, LANES), lambda i: (0, off + i))
    return pl.pallas_call(
        body, name="conv_module", grid=(nt,),
        in_specs=[seq(v0), seq(g0), pl.BlockSpec((CONV_WIDTH, LANES), lambda i: (0, i)), pl.BlockSpec((1, LANES), lambda i: (0, i))],
        out_specs=seq(0), out_shape=jax.ShapeDtypeStruct((S, C), F32),
        scratch_shapes=[pltpu.VMEM((S + CONV_PAD, LANES), F32)],
        compiler_params=_params(("parallel",)),
    )(proj, proj, w, b)


def _conv_bwd_call(proj, col0, w, dc1, dproj):
    S = proj.shape[0]
    C = w.shape[1]
    nt = C // LANES
    v0, g0 = col0 // LANES, (col0 + C) // LANES

    def body(val_ref, gate_ref, w_ref, dy_ref, _, dproj_ref, dw_ref, db_ref, xpad_ref, tail_ref, dwacc_ref, stage_ref, sems):
        i = pl.program_id(0)
        xpad_ref[0:CONV_PAD, :] = jnp.zeros((CONV_PAD, LANES), F32)
        xpad_ref[CONV_PAD:, :] = val_ref[...] * _sigmoid(gate_ref[...])
        _fill_tail(tail_ref, dy_ref, CONV_PAD)
        dwacc_ref[...] = jnp.zeros_like(dwacc_ref)

        def fill(slot):
            def block(t0, dy_src, dy_t0):
                rows = pl.ds(t0, TIME_BLOCK)
                _conv_weight_grad(xpad_ref, t0, dy_ref[rows, :], dwacc_ref, CONV_WIDTH, CONV_PAD)
                dc0 = _conv_transpose_block(dy_src, dy_t0, w_ref, CONV_WIDTH)
                sg = _sigmoid(gate_ref[rows, :])
                stage_ref[slot, 0, rows, :] = (dc0 * sg).astype(BF16)
                stage_ref[slot, 1, rows, :] = (dc0 * val_ref[rows, :] * sg * (1.0 - sg)).astype(BF16)

            _time_loop(S, lambda t0: block(t0, dy_ref, t0), skip_last=1)
            block(S - TIME_BLOCK, tail_ref, 0)

        _staged_window_stores(stage_ref, dproj_ref, sems, i, nt, pl.ds(0, S),
                              [col0 + i * LANES, col0 + C + i * LANES], fill)
        dw_ref[...] = dwacc_ref[...]
        db_ref[...] = jnp.sum(dy_ref[...], axis=0, keepdims=True)

    seq = lambda off: pl.BlockSpec((S, LANES), lambda i: (0, off + i))
    return pl.pallas_call(
        body, name="conv_module_bwd", grid=(nt,),
        in_specs=[seq(v0), seq(g0), pl.BlockSpec((CONV_WIDTH, LANES), lambda i: (0, i)), seq(0), ANY],
        out_specs=[ANY, pl.BlockSpec((CONV_PAD, LANES), lambda i: (0, i)), pl.BlockSpec((1, LANES), lambda i: (0, i))],
        out_shape=[jax.ShapeDtypeStruct(dproj.shape, BF16),
                   jax.ShapeDtypeStruct((CONV_PAD, C), F32), jax.ShapeDtypeStruct((1, C), F32)],
        input_output_aliases={4: 0},
        scratch_shapes=[pltpu.VMEM((S + CONV_PAD, LANES), F32), pltpu.VMEM((TIME_BLOCK + CONV_PAD, LANES), F32),
                        pltpu.VMEM((CONV_PAD, LANES), F32),
                        pltpu.VMEM((2, 2, S, LANES), BF16), pltpu.SemaphoreType.DMA((2, 2))],
        compiler_params=_params(("arbitrary",)),
    )(proj, proj, w, dc1, dproj)


def _ffn_fwd_call(u, w, b):
    S, C2 = u.shape
    C = C2 // 2
    nt = C // LANES

    def body(ug_ref, uv_ref, wg_ref, wv_ref, bg_ref, bv_ref, f_ref, hg_ref, hv_ref):
        _fill_head(hg_ref, ug_ref, FFN_PAD)
        _fill_head(hv_ref, uv_ref, FFN_PAD)

        def block(t0, xg_ref, xv_ref, x_t0, pad):
            cg = _conv_block(xg_ref, x_t0, wg_ref, FFN_CONV_WIDTH, pad) + bg_ref[...]
            cv = _conv_block(xv_ref, x_t0, wv_ref, FFN_CONV_WIDTH, pad) + bv_ref[...]
            f_ref[pl.ds(t0, TIME_BLOCK), :] = (_gelu(cg) * cv).astype(BF16)

        block(0, hg_ref, hv_ref, 0, FFN_PAD)
        _time_loop(S, lambda t0: block(t0, ug_ref, uv_ref, t0, 0), skip_first=1)

    seq = lambda off: pl.BlockSpec((S, LANES), lambda i: (0, off + i))
    wsp = lambda off: pl.BlockSpec((FFN_CONV_WIDTH, LANES), lambda i: (0, off + i))
    bsp = lambda off: pl.BlockSpec((1, LANES), lambda i: (0, off + i))
    return pl.pallas_call(
        body, name="ffn_conv_geglu", grid=(nt,),
        in_specs=[seq(0), seq(nt), wsp(0), wsp(nt), bsp(0), bsp(nt)],
        out_specs=seq(0), out_shape=jax.ShapeDtypeStruct((S, C), BF16),
        scratch_shapes=[pltpu.VMEM((FFN_PAD + TIME_BLOCK, LANES), F32)] * 2,
        compiler_params=_params(("parallel",)),
    )(u, u, w, w, b, b)


def _ffn_bwd_call(u, w, b, df):
    S, C2 = u.shape
    C = C2 // 2
    nt = C // LANES

    def body(ug_ref, uv_ref, wg_ref, wv_ref, bg_ref, bv_ref, df_ref,
             du_ref, dwg_ref, dwv_ref, dbg_ref, dbv_ref,
             hg_ref, hv_ref, dg_ref, dv_ref, dwg_acc, dwv_acc, dbg_acc, dbv_acc):
        zeros = jnp.zeros((FFN_PAD, LANES), F32)
        _fill_head(hg_ref, ug_ref, FFN_PAD)
        _fill_head(hv_ref, uv_ref, FFN_PAD)
        dg_ref[S:, :] = zeros
        dv_ref[S:, :] = zeros
        dwg_acc[...] = jnp.zeros_like(dwg_acc)
        dwv_acc[...] = jnp.zeros_like(dwv_acc)
        dbg_acc[...] = jnp.zeros_like(dbg_acc)
        dbv_acc[...] = jnp.zeros_like(dbv_acc)

        def first(t0, xg_ref, xv_ref, x_t0, pad):
            rows = pl.ds(t0, TIME_BLOCK)
            cg = _conv_block(xg_ref, x_t0, wg_ref, FFN_CONV_WIDTH, pad) + bg_ref[...]
            cv = _conv_block(xv_ref, x_t0, wv_ref, FFN_CONV_WIDTH, pad) + bv_ref[...]
            dfb = df_ref[rows, :]
            gelu, gelu_grad = _gelu_and_grad(cg)
            dcg = dfb * cv * gelu_grad
            dcv = dfb * gelu
            dg_ref[rows, :] = dcg
            dv_ref[rows, :] = dcv
            _conv_weight_grad(xg_ref, x_t0, dcg, dwg_acc, FFN_CONV_WIDTH, pad)
            _conv_weight_grad(xv_ref, x_t0, dcv, dwv_acc, FFN_CONV_WIDTH, pad)
            dbg_acc[...] += jnp.sum(dcg, axis=0, keepdims=True)
            dbv_acc[...] += jnp.sum(dcv, axis=0, keepdims=True)

        def second(t0):
            rows = pl.ds(t0, TIME_BLOCK)
            du_ref[0, rows, :] = _conv_transpose_block(dg_ref, t0, wg_ref, FFN_CONV_WIDTH).astype(BF16)
            du_ref[1, rows, :] = _conv_transpose_block(dv_ref, t0, wv_ref, FFN_CONV_WIDTH).astype(BF16)

        first(0, hg_ref, hv_ref, 0, FFN_PAD)
        _time_loop(S, lambda t0: first(t0, ug_ref, uv_ref, t0, 0), skip_first=1)
        _time_loop(S, second)
        dwg_ref[...] = dwg_acc[...]
        dwv_ref[...] = dwv_acc[...]
        dbg_ref[...] = dbg_acc[...]
        dbv_ref[...] = dbv_acc[...]

    seq = lambda off: pl.BlockSpec((S, LANES), lambda i: (0, off + i))
    wsp = lambda off: pl.BlockSpec((FFN_CONV_WIDTH, LANES), lambda i: (0, off + i))
    bsp = lambda off: pl.BlockSpec((1, LANES), lambda i: (0, off + i))
    return pl.pallas_call(
        body, name="ffn_conv_geglu_bwd", grid=(nt,),
        in_specs=[seq(0), seq(nt), wsp(0), wsp(nt), bsp(0), bsp(nt), seq(0)],
        out_specs=[pl.BlockSpec((2, S, LANES), lambda i: (0, 0, i)),
                   pl.BlockSpec((SUBLANES, LANES), lambda i: (0, i)), pl.BlockSpec((SUBLANES, LANES), lambda i: (0, i)),
                   bsp(0), bsp(0)],
        out_shape=[jax.ShapeDtypeStruct((2, S, C), BF16)] + [jax.ShapeDtypeStruct((SUBLANES, C), F32)] * 2
        + [jax.ShapeDtypeStruct((1, C), F32)] * 2,
        scratch_shapes=[pltpu.VMEM((FFN_PAD + TIME_BLOCK, LANES), F32)] * 2 + [pltpu.VMEM((S + FFN_PAD, LANES), F32)] * 2
        + [pltpu.VMEM((SUBLANES, LANES), F32)] * 2
        + [pltpu.VMEM((1, LANES), F32)] * 2,
        compiler_params=_params(("parallel",)),
    )(u, u, w, w, b, b, df)


def _adamw(w_ref, g_ref, m_ref, v_ref, d_ref, mo_ref, vo_ref):
    gv = g_ref[...]
    mn = ADAM_B1 * m_ref[...] + (1.0 - ADAM_B1) * gv
    vn = ADAM_B2 * v_ref[...] + (1.0 - ADAM_B2) * (gv * gv)
    mo_ref[...] = mn
    vo_ref[...] = vn
    m_hat = mn * (1.0 / (1.0 - ADAM_B1 ** ADAM_STEP))
    v_hat = vn * (1.0 / (1.0 - ADAM_B2 ** ADAM_STEP))
    d_ref[...] = -ADAM_LR * (m_hat / (jnp.sqrt(v_hat) + ADAM_EPS) + ADAM_WD * w_ref[...])


def _adamw_call(w, g, m, v, name):
    R, C = w.shape
    tr = _row_tile(R, C)

    def body(w_ref, g_ref, m_ref, v_ref, go_ref, d_ref, mo_ref, vo_ref):
        go_ref[...] = g_ref[...]
        _adamw(w_ref, g_ref, m_ref, v_ref, d_ref, mo_ref, vo_ref)

    spec = pl.BlockSpec((tr, C), lambda i: (i, 0))
    return pl.pallas_call(
        body, name=name, grid=(R // tr,),
        in_specs=[spec] * 4, out_specs=[spec] * 4,
        out_shape=[jax.ShapeDtypeStruct((R, C), F32)] * 4,
        compiler_params=_params(("parallel",)),
    )(w, g, m, v)


def _adamw_small_call(ws, gs, ms, vs):
    n = len(ws)

    def body(*refs):
        w_refs, g_refs, m_refs, v_refs, d_refs, mo_refs, vo_refs = (refs[i * n:(i + 1) * n] for i in range(7))
        for i in range(n):
            _adamw(w_refs[i], g_refs[i], m_refs[i], v_refs[i], d_refs[i], mo_refs[i], vo_refs[i])

    whole = pl.BlockSpec(memory_space=pltpu.VMEM)
    outs = pl.pallas_call(
        body, name="adamw_small",
        in_specs=[whole] * (4 * n), out_specs=[whole] * (3 * n),
        out_shape=[jax.ShapeDtypeStruct(w.shape, F32) for w in ws] * 3,
    )(*ws, *gs, *ms, *vs)
    return outs[:n], outs[n:2 * n], outs[2 * n:]


def _position():
    return lax.axis_index("x"), lax.axis_index("y"), lax.axis_index("c")


def _chip_peers(x, y):
    return [(x, 1 - y), (1 - x, y), (1 - x, 1 - y)]


def _half_rows(ref, core, rows):
    h = rows // 2
    start = pl.multiple_of(core * h, PACKED_ROWS)
    return ref.at[pl.ds(start, h), :] if len(ref.shape) == 2 else ref.at[:, pl.ds(start, h), :]


def _shard_half(ref, shard, core, rows):
    h = rows // 2
    return ref.at[shard, pl.ds(pl.multiple_of(core * h, PACKED_ROWS), h), :]


ANY = pl.BlockSpec(memory_space=pl.ANY)


def _first_hop_copies(srcs, lands):
    x, y, c = _position()
    chip = 2 * x + y
    targets = [(px, py, c) for px, py in _chip_peers(x, y)] + [(x, y, 1 - c)]
    rows = srcs[0].shape[0]
    out = []
    for i, (s, l) in enumerate(zip(srcs, lands)):
        for k, dev in enumerate(targets):
            if i == 0 and k < 3:
                out.append((_half_rows(s, c, rows), _shard_half(l, chip, c, rows), dev, k))
            else:
                out.append((s, l.at[chip], dev, len(targets) * i + k))
    return out


def _second_hop_copies(srcs, lands):
    x, y, c = _position()
    rows = lands[0].shape[1]
    out = []
    for k, (px, py) in enumerate(_chip_peers(x, y)):
        half = _shard_half(lands[0], 2 * px + py, c, rows)
        out.append((half, half, (x, y, 1 - c), k))
    return out


HBM_SPEC = pl.BlockSpec(memory_space=pltpu.HBM)
SEM_SPEC = pl.BlockSpec(memory_space=pltpu.SEMAPHORE)
DATAFLOW = pltpu.SideEffectType.DATAFLOW_SIDE_EFFECTING


def _in_hbm(a):
    return pltpu.with_memory_space_constraint(a, pltpu.HBM)


def _split_start(name, groups, after, carry=None):
    spans, arrays = [], []
    for srcs, lands, _, _ in groups:
        spans.append((len(arrays), len(srcs), len(lands)))
        arrays += list(srcs) + list(lands)
    if carry is not None:
        arrays.append(carry)
    na, ng = len(arrays), len(groups)

    def body(*refs):
        sems, token = refs[na + 1:na + 1 + 2 * ng], refs[-1]
        for g, (_, _, _, copies) in enumerate(groups):
            off, ns, nl = spans[g]
            for src, dst, dev, idx in copies(refs[off:off + ns], refs[off + ns:off + ns + nl]):
                pltpu.make_async_remote_copy(src_ref=src, dst_ref=dst, send_sem=sems[2 * g].at[idx], recv_sem=sems[2 * g + 1].at[idx],
                                             device_id=dev, device_id_type=MESH).start()
        token[...] = jnp.zeros_like(token)

    outs = pl.pallas_call(
        body, name=name,
        in_specs=[HBM_SPEC] * na + [ANY],
        out_specs=[SEM_SPEC] * (2 * ng) + [HBM_SPEC] * na + [pl.BlockSpec(memory_space=pltpu.VMEM)],
        out_shape=[pltpu.SemaphoreType.DMA((n_sems,)) for _, _, n_sems, _ in groups for _ in range(2)]
        + [pltpu.HBM(a.shape, a.dtype) for a in arrays] + [jax.ShapeDtypeStruct((SUBLANES, LANES), F32)],
        input_output_aliases={i: 2 * ng + i for i in range(na)},
        compiler_params=pltpu.CompilerParams(has_side_effects=DATAFLOW),
    )(*[_in_hbm(a) for a in arrays], after)
    started = []
    for g, (off, ns, nl) in enumerate(spans):
        thru = outs[2 * ng + off:2 * ng + off + ns + nl]
        started.append(dict(send=outs[2 * g], recv=outs[2 * g + 1], srcs=list(thru[:ns]), lands=list(thru[ns:]),
                            tile=outs[-1], token=outs[-1][0, 0], carry=None if carry is None else outs[2 * ng + na - 1]))
    return started


def _split_wait(name, started, copies, after):
    n, m = len(started["srcs"]), len(started["lands"])

    def body(*refs):
        src_refs, land_refs = refs[:n], refs[n:n + m]
        send_sem, recv_sem = refs[n + m], refs[n + m + 1]
        for src, dst, dev, idx in copies(src_refs, land_refs):
            cp = pltpu.make_async_remote_copy(src_ref=src, dst_ref=dst, send_sem=send_sem.at[idx], recv_sem=recv_sem.at[idx],
                                              device_id=dev, device_id_type=MESH)
            cp.wait_send()
            cp.wait_recv()

    arrays = started["srcs"] + started["lands"]
    outs = pl.pallas_call(
        body, name=name,
        in_specs=[HBM_SPEC] * (n + m) + [SEM_SPEC, SEM_SPEC, ANY],
        out_specs=[HBM_SPEC] * (n + m),
        out_shape=[pltpu.HBM(a.shape, a.dtype) for a in arrays],
        input_output_aliases={i: i for i in range(n + m)},
        compiler_params=pltpu.CompilerParams(has_side_effects=DATAFLOW),
    )(*arrays, started["send"], started["recv"], after)
    return list(outs)


def _gather_copies(srcs, lands):
    x, y, c = _position()
    chip = 2 * x + y
    targets = [(px, py, c) for px, py in _chip_peers(x, y)] + [(x, y, 1 - c)]
    return [(s, l.at[chip], dev, len(targets) * i + k) for i, (s, l) in enumerate(zip(srcs, lands)) for k, dev in enumerate(targets)]


def _sibling_copies(srcs, lands):
    x, y, c = _position()
    return [(_half_rows(srcs[0], 1 - c, srcs[0].shape[1]), lands[0], (x, y, 1 - c), 0)]


def _sibling_whole_copies(srcs, lands):
    x, y, c = _position()
    return [(srcs[0], lands[0], (x, y, 1 - c), 0)]


def _exchange_copies(srcs, lands):
    x, y, c = _position()
    return [(srcs[0].at[2 * px + py], lands[0].at[k], (px, py, c), k) for k, (px, py) in enumerate(_chip_peers(x, y))]


def _pair_sum_call(grad, recv, chip_core, name):
    _, h, B = recv.shape
    tr = _row_tile(h, B)

    def body(cc_ref, g_ref, r_ref, o_ref, ob_ref):
        s = g_ref[...] + r_ref[...]
        ob_ref[...] = s.astype(BF16)

        @pl.when(pl.program_id(1) == cc_ref[0])
        def _():
            o_ref[...] = s

    g_spec = pl.BlockSpec((None, tr, B), lambda i, q, cc_ref: (q, cc_ref[1] * (h // tr) + i, 0))
    spec = pl.BlockSpec((None, tr, B), lambda i, q, cc_ref: (q, i, 0))
    own_spec = pl.BlockSpec((tr, B), lambda i, q, cc_ref: (i, 0))
    return pl.pallas_call(
        body, name=name,
        grid_spec=pltpu.PrefetchScalarGridSpec(num_scalar_prefetch=1, grid=(h // tr, N_CHIPS), in_specs=[g_spec, spec],
                                               out_specs=[own_spec, spec]),
        out_shape=[jax.ShapeDtypeStruct((h, B), F32), jax.ShapeDtypeStruct(recv.shape, BF16)],
        compiler_params=_params(("parallel", "arbitrary")),
    )(chip_core, grad, recv)


def _chip_sum_call(partial, recv, chip_core, name):
    _, h, B = recv.shape
    tr = _row_tile(h, B)

    def body(cc_ref, p_ref, r_ref, o_ref):
        o_ref[...] = ((p_ref[...] + r_ref[0].astype(F32)) + r_ref[1].astype(F32)) + r_ref[2].astype(F32)

    return pl.pallas_call(
        body, name=name,
        grid_spec=pltpu.PrefetchScalarGridSpec(
            num_scalar_prefetch=1, grid=(h // tr,),
            in_specs=[pl.BlockSpec((tr, B), lambda i, cc_ref: (i, 0)),
                      pl.BlockSpec((3, tr, B), lambda i, cc_ref: (0, i, 0))],
            out_specs=pl.BlockSpec((tr, B), lambda i, cc_ref: (cc_ref[1] * (h // tr) + i, 0))),
        out_shape=jax.ShapeDtypeStruct((2 * h, B), F32),
        compiler_params=_params(("parallel",)),
    )(chip_core, partial, recv)


def _sibling_assemble_call(shards, name="grad_sibling_assemble"):
    n = len(shards)

    def body(*refs):
        ins, outs = refs[:n], refs[n:2 * n]
        send_sems, recv_sems = refs[2 * n:]
        x, y, c = _position()
        copies = []
        for i in range(n):
            rows = shards[i].shape[0]
            cp = pltpu.make_async_remote_copy(src_ref=_half_rows(ins[i], c, rows), dst_ref=_half_rows(outs[i], c, rows),
                                              send_sem=send_sems.at[i], recv_sem=recv_sems.at[i],
                                              device_id=(x, y, 1 - c), device_id_type=MESH)
            cp.start()
            copies.append(cp)
        for cp in copies:
            cp.wait()

    return pl.pallas_call(
        body, name=name,
        in_specs=[ANY] * n, out_specs=[ANY] * n,
        out_shape=[jax.ShapeDtypeStruct(s.shape, F32) for s in shards],
        input_output_aliases={i: i for i in range(n)},
        scratch_shapes=[pltpu.SemaphoreType.DMA((n,)), pltpu.SemaphoreType.DMA((n,))],
    )(*shards)


N_DEVICES = 8


def _allsum_copies(srcs, lands):
    x, y, c = _position()
    me = 4 * x + 2 * y + c
    out = []
    for k in range(1, N_DEVICES):
        peer = (1 - x if k & 4 else x, 1 - y if k & 2 else y, 1 - c if k & 1 else c)
        out.append((srcs[0], lands[0].at[me], peer, k - 1))
    return out


def _ordered_sum_call(mine, landed, me_chip, shapes, sharded_cols):
    rows = mine.shape[0]
    outs = [(s[0], n) if n else s for s, n in zip(shapes, sharded_cols)]

    def body(mc_ref, x_ref, l_ref, *refs):
        acc_ref = refs[-1]
        acc = jnp.where(mc_ref[0] == 0, x_ref[...], l_ref[0])
        for d in range(1, N_DEVICES):
            acc = acc + jnp.where(mc_ref[0] == d, x_ref[...], l_ref[d])
        acc_ref[...] = acc
        first = 0
        for o_ref, (r, c), n in zip(refs[:-1], shapes, sharded_cols):
            per_row = c // LANES

            def unpack(chip, o_ref=o_ref, r=r, n=n, per_row=per_row, first=first):
                for i in range(r):
                    for j in range((n or per_row * LANES) // LANES):
                        src = first + i * per_row + chip * ((n or 0) // LANES) + j
                        o_ref[i:i + 1, j * LANES:(j + 1) * LANES] = acc_ref[src:src + 1, :]

            if n:
                for q in range(N_CHIPS):
                    pl.when(mc_ref[1] == q)(functools.partial(unpack, q))
            else:
                unpack(0)
            first += r * per_row

    results = pl.pallas_call(
        body, name="small_grad_sum",
        in_specs=[pl.BlockSpec(memory_space=pltpu.SMEM), pl.BlockSpec(memory_space=pltpu.VMEM), pl.BlockSpec(memory_space=pltpu.VMEM)],
        out_specs=[pl.BlockSpec(memory_space=pltpu.VMEM)] * len(outs),
        out_shape=[jax.ShapeDtypeStruct(s, F32) for s in outs],
        scratch_shapes=[pltpu.VMEM((rows, LANES), F32)],
    )(me_chip, mine, landed)
    return results


def _pack(arrays):
    flat = jnp.concatenate([a.reshape(-1).astype(F32) for a in arrays])
    rows = -(-flat.shape[0] // LANES)
    rows = -(-rows // SUBLANES) * SUBLANES
    flat = jnp.pad(flat, (0, rows * LANES - flat.shape[0]))
    return flat.reshape(rows, LANES)


def _local_step(xs, target, P, late_weights, on_grad):
    S, D = xs.shape
    qkv_width = 3 * N_HEADS * HEAD_DIM
    glu_col0, gate_col0 = qkv_width, qkv_width + 2 * D
    shard_major = lambda g: g.reshape(N_CHIPS, g.shape[0] // N_CHIPS, g.shape[1])

    h1 = _rms_fwd_call(xs, P["norm_mix_pre"])
    buckets = _bucket_tables()
    bias = _bias_table_call(P["rel_bias"] + 0.0 * h1[0, 0].astype(F32), buckets)
    P = dict(P, **late_weights("in", bias))
    proj = _matmul(h1, P["w_in"], "nn", "proj_in")
    parts = []
    for g in range(N_GROUPS):
        parts += _attn_fwd_call(proj, bias, g)
    a, a_bf, lse = _attn_merge_call(parts)
    P = dict(P, **late_weights("mix", a_bf))
    y_a = _matmul(a_bf, P["w_attn_out"], "nn", "attn_out")
    c1 = _conv_fwd_call(proj, glu_col0, P["conv_dw_w"], P["conv_dw_b"])
    cact = _ln_silu_call(c1, P["conv_ln_g"], P["conv_ln_b"])
    y_c = _matmul(cact, P["conv_pw_w"], "nn", "conv_pw")
    mixed = _mix_call(proj, gate_col0, P["b_gate"], y_a, y_c)
    out = _matmul(mixed, P["w_out"], "nn", "mix_out")
    x1, h2 = _res1_call(xs, out, P["norm_mix_post"], P["norm_ffn_pre"])
    P = dict(P, **late_weights("up", h2))
    u = _matmul(h2, P["w_up"], "nn", "ffn_up")
    f = _ffn_fwd_call(u, P["ffn_conv_w"], P["ffn_conv_b"])
    P = dict(P, **late_weights("down", f))
    yff = _matmul(f, P["w_down"], "nn", "ffn_down")
    loss_tile, dx2, dyff, dg_ffn_post = _loss_call(yff, x1, P["norm_ffn_post"], target)

    G = {}
    G["norm_ffn_post"] = dg_ffn_post
    on_grad("w_down", shard_major(_matmul(f, dyff, "tn", "ffn_down_dw")))
    df = _matmul(dyff, P["w_down"], "nt", "ffn_down_dx")
    du, dwg, dwv, dbg, dbv = _ffn_bwd_call(u, P["ffn_conv_w"], P["ffn_conv_b"], df)
    G["ffn_conv_w"] = jnp.concatenate([dwg[:FFN_CONV_WIDTH], dwv[:FFN_CONV_WIDTH]], axis=1)
    G["ffn_conv_b"] = jnp.concatenate([dbg, dbv], axis=1)
    du = on_grad("w_up", functools.partial(_grad_half_matmul, h2, "ffn_up_dw"), carry=du)
    dh2 = _matmul(du, P["w_up"], "nt", "ffn_up_dx")
    dx1, dout, G["norm_ffn_pre"], G["norm_mix_post"] = _mid_bwd_call(x1, P["norm_ffn_pre"], dh2, dx2, out, P["norm_mix_post"])
    on_grad("w_out", shard_major(_matmul(mixed, dout, "tn", "mix_out_dw")))
    dmixed = _matmul(dout, P["w_out"], "nt", "mix_out_dx")
    dya, dyc, dproj, dba, dbc = _mix_bwd_call(dmixed, proj, gate_col0, P["b_gate"], y_a, y_c)
    G["b_gate"] = jnp.concatenate([dba, dbc], axis=1)
    on_grad("w_attn_out", _matmul(a_bf, dya, "tn", "attn_out_dw", out_shards=True))
    dyc = on_grad("conv_pw_w", shard_major(_matmul(cact, dyc, "tn", "conv_pw_dw")), carry=dyc)
    da = _matmul(dya, P["w_attn_out"], "nt", "attn_out_dx")
    dcact = _matmul(dyc, P["conv_pw_w"], "nt", "conv_pw_dx")
    dc1, G["conv_ln_g"], G["conv_ln_b"] = _ln_silu_bwd_call(c1, P["conv_ln_g"], P["conv_ln_b"], dcact)
    dproj, dw_dw, G["conv_dw_b"] = _conv_bwd_call(proj, glu_col0, P["conv_dw_w"], dc1, dproj)
    G["conv_dw_w"] = dw_dw[:CONV_WIDTH]
    delta = _attn_delta_call(a, da)
    dqs, dks, dvs, dbs = [], [], [], []
    for g in range(N_GROUPS):
        dq, dk, dv, db = _attn_bwd_call(proj, bias, da, lse, delta, g)
        dqs.append(dq)
        dks.append(dk)
        dvs.append(dv)
        dbs.append(db)
    G["rel_bias"] = _bias_grad_call(jnp.concatenate(dbs, axis=0), buckets)
    dproj = _dproj_call(dqs + dks + dvs, dproj)
    dproj = on_grad("w_in", functools.partial(_grad_half_matmul, h1, "proj_in_dw"), carry=dproj)
    dh1 = _matmul(dproj, P["w_in"], "nt", "proj_in_dx")
    dh1 = on_grad(None, None, carry=dh1)
    grad_x, G["norm_mix_pre"] = _in_bwd_call(xs, P["norm_mix_pre"], dh1, dx1)
    return loss_tile, grad_x, G


def kernel(x, w_in, b_gate, rel_bias, w_attn_out, conv_dw_w, conv_dw_b, conv_ln_g, conv_ln_b, conv_pw_w, w_out, norm_mix_pre, norm_mix_post, norm_ffn_pre, norm_ffn_post, w_up, ffn_conv_w, ffn_conv_b, w_down, loss_target, m_w_in, m_b_gate, m_rel_bias, m_w_attn_out, m_conv_dw_w, m_conv_dw_b, m_conv_ln_g, m_conv_ln_b, m_conv_pw_w, m_w_out, m_norm_mix_pre, m_norm_mix_post, m_norm_ffn_pre, m_norm_ffn_post, m_w_up, m_ffn_conv_w, m_ffn_conv_b, m_w_down, v_w_in, v_b_gate, v_rel_bias, v_w_attn_out, v_conv_dw_w, v_conv_dw_b, v_conv_ln_g, v_conv_ln_b, v_conv_pw_w, v_w_out, v_norm_mix_pre, v_norm_mix_post, v_norm_ffn_pre, v_norm_ffn_post, v_w_up, v_ffn_conv_w, v_ffn_conv_b, v_w_down):
    weights = dict(w_in=w_in, b_gate=b_gate, rel_bias=rel_bias, w_attn_out=w_attn_out, conv_dw_w=conv_dw_w, conv_dw_b=conv_dw_b,
                   conv_ln_g=conv_ln_g, conv_ln_b=conv_ln_b, conv_pw_w=conv_pw_w, w_out=w_out, norm_mix_pre=norm_mix_pre,
                   norm_mix_post=norm_mix_post, norm_ffn_pre=norm_ffn_pre, norm_ffn_post=norm_ffn_post, w_up=w_up,
                   ffn_conv_w=ffn_conv_w, ffn_conv_b=ffn_conv_b, w_down=w_down)
    m_in = dict(w_in=m_w_in, b_gate=m_b_gate, rel_bias=m_rel_bias, w_attn_out=m_w_attn_out, conv_dw_w=m_conv_dw_w,
                conv_dw_b=m_conv_dw_b, conv_ln_g=m_conv_ln_g, conv_ln_b=m_conv_ln_b, conv_pw_w=m_conv_pw_w, w_out=m_w_out,
                norm_mix_pre=m_norm_mix_pre, norm_mix_post=m_norm_mix_post, norm_ffn_pre=m_norm_ffn_pre,
                norm_ffn_post=m_norm_ffn_post, w_up=m_w_up, ffn_conv_w=m_ffn_conv_w, ffn_conv_b=m_ffn_conv_b, w_down=m_w_down)
    v_in = dict(w_in=v_w_in, b_gate=v_b_gate, rel_bias=v_rel_bias, w_attn_out=v_w_attn_out, conv_dw_w=v_conv_dw_w,
                conv_dw_b=v_conv_dw_b, conv_ln_g=v_conv_ln_g, conv_ln_b=v_conv_ln_b, conv_pw_w=v_conv_pw_w, w_out=v_w_out,
                norm_mix_pre=v_norm_mix_pre, norm_mix_post=v_norm_mix_post, norm_ffn_pre=v_norm_ffn_pre,
                norm_ffn_post=v_norm_ffn_post, w_up=v_w_up, ffn_conv_w=v_ffn_conv_w, ffn_conv_b=v_ffn_conv_b, w_down=v_w_down)
    names = list(weights)
    xi, yi, ci = _position()
    chip = 2 * xi + yi
    core_arr = jnp.reshape(ci, (1,)).astype(jnp.int32)

    xs = x[0]
    target = loss_target[0]
    S, D = xs.shape

    big = ["w_in", "w_attn_out", "conv_pw_w", "w_out", "w_up", "w_down"]
    row_sharded = ("conv_pw_w", "w_out", "w_down")
    natural = lambda k, g: g.reshape(-1, g.shape[2]) if k in row_sharded else g
    first_srcs = [w_in[0].astype(BF16), conv_dw_w[0], ffn_conv_w[0]]
    first_lands = [lax.empty((N_CHIPS,) + s.shape, s.dtype) for s in first_srcs]
    (first_hop,) = _split_start("gather_in_start", [(first_srcs, first_lands, 4 * len(first_srcs), _first_hop_copies)], core_arr)
    launched = first_hop["token"]
    late_sets = dict(mix=["w_attn_out", "conv_pw_w", "w_out"], up=["w_up"], down=["w_down"])
    late_groups = []
    for keys in late_sets.values():
        srcs = [(weights[k][0] + launched).astype(BF16) for k in keys]
        late_groups.append((srcs, [lax.empty((N_CHIPS,) + s.shape, BF16) for s in srcs], 4 * len(keys), _gather_copies))
    started = {}

    def late_weights(tag, after):
        if tag == "in":
            w_in_halves, dw4, fc4 = _split_wait("gather_in_wait", first_hop, _first_hop_copies, after)[len(first_srcs):]
            second_hop, *late = _split_start("gather_in_pass_start", [([], [w_in_halves], 3, _second_hop_copies)] + late_groups, dw4)
            started.update(zip(late_sets, late))
            (w_in_full,) = _split_wait("gather_in_pass_wait", second_hop, _second_hop_copies, second_hop["tile"])
            return dict(w_in=w_in_full, conv_dw_w=jnp.concatenate(list(dw4), axis=1), ffn_conv_w=jnp.concatenate(list(fc4), axis=1))
        landed = _split_wait(f"gather_{tag}_wait", started[tag], _gather_copies, after)[len(late_sets[tag]):]
        return {k: natural(k, g) for k, g in zip(late_sets[tag], landed)}

    chip_core = jnp.stack([chip, ci]).astype(jnp.int32)
    exchanging, pending, second_half = {}, {}, {}

    held = []

    def launch(tag, after, carry=None):
        keys, groups, partial = [], [], {}
        for k in list(exchanging):
            st, copies = exchanging.pop(k)
            gk, r1 = _split_wait(f"sibling_exchange_wait_{k}", st, copies, after)
            if k in second_half:
                partial[k], s16 = second_half.pop(k)(init=r1)
            else:
                partial[k], s16 = _pair_sum_call(gk, r1, chip_core, f"pair_sum_{k}")
            keys.append(k)
            groups.append(([s16], [lax.empty((3,) + s16.shape[1:], BF16)], 3, _exchange_copies))
        fresh = [(k, copies) for k, _, copies in held]
        for _, g3, copies in held:
            rows = g3.shape[1] // 2 if copies is _sibling_copies else g3.shape[1]
            groups.append(([g3], [lax.empty((N_CHIPS, rows, g3.shape[2]), F32)], 1, copies))
        held.clear()
        begun = _split_start(f"grad_exchange_start_{tag}", groups, core_arr, carry)
        for k, st in zip(keys, begun):
            pending[k] = (partial[k], st)
        for (k, copies), st in zip(fresh, begun[len(keys):]):
            exchanging[k] = (st, copies)
        return begun[0]["carry"]

    def on_grad(k, g, carry=None):
        if k is None:
            return launch("last", carry[:SUBLANES, :LANES], carry)
        if callable(g):
            theirs = g(jnp.stack([chip, 1 - ci]).astype(jnp.int32), carry)
            held.append((k, theirs, _sibling_whole_copies))
            carried = launch(k, theirs[0, :SUBLANES, :LANES], carry)
            second_half[k] = functools.partial(g, chip_core, carried)
            return carried
        held.append((k, g, _sibling_copies))
        if k in ("w_down", "w_out", "w_attn_out"):
            return carry
        return launch(k, g[0, :SUBLANES, :LANES], carry)

    def finish(keys, after, tag):
        halves = []
        for k in keys:
            s32, st = pending[k]
            recv2 = _split_wait(f"chip_exchange_wait_{k}", st, _exchange_copies, after)[1]
            halves.append(_chip_sum_call(s32, recv2, chip_core, f"chip_sum_{k}"))
        return dict(zip(keys, _sibling_assemble_call(halves, f"grad_sibling_assemble_{tag}")))

    P = dict(b_gate=b_gate, rel_bias=rel_bias, conv_dw_b=conv_dw_b, conv_ln_g=conv_ln_g, conv_ln_b=conv_ln_b,
             norm_mix_pre=norm_mix_pre + launched, norm_mix_post=norm_mix_post, norm_ffn_pre=norm_ffn_pre,
             norm_ffn_post=norm_ffn_post, ffn_conv_b=ffn_conv_b)
    loss_tile, grad_x, G = _local_step(xs, target, P, late_weights, on_grad)

    small = [k for k in names if k not in big]
    packed = _pack([loss_tile[:1]] + [G[k] for k in small])
    (allsum,) = _split_start("small_grad_allsum_start",
                             [([packed], [jnp.zeros((N_DEVICES,) + packed.shape, F32)], N_DEVICES - 1, _allsum_copies)], core_arr)

    reduced, grads, deltas, new_m, new_v = {}, {}, {}, {}, {}

    def update(keys):
        for k in keys:
            gk, d, mn, vn = _adamw_call(weights[k][0], reduced[k], m_in[k][0], v_in[k][0], f"adamw_{k}")
            grads[k], deltas[k], new_m[k], new_v[k] = gk[None], d[None], mn[None], vn[None]

    others = [k for k in big if k != "w_in"]
    reduced.update(finish(others, allsum["tile"], "others"))
    update(others)
    reduced.update(finish(["w_in"], deltas["w_up"], "w_in"))
    update(["w_in"])

    me_chip = jnp.stack([4 * xi + 2 * yi + ci, chip]).astype(jnp.int32)
    mine, landed = _split_wait("small_grad_allsum_wait", allsum, _allsum_copies, deltas["w_in"])
    piece_shapes = [(1, LANES)] + [(G[k].size // LANES, LANES) if k == "rel_bias" else G[k].shape for k in small]
    piece_cols = [0] + [weights[k].shape[2] if k in ("conv_dw_w", "ffn_conv_w") else 0 for k in small]
    loss_row, *summed = _ordered_sum_call(mine, landed, me_chip, piece_shapes, piece_cols)
    loss = loss_row[0, 0]
    for k, gsum in zip(small, summed):
        grads[k] = gsum.reshape(weights[k].shape)
    ds, mns, vns = _adamw_small_call([weights[k] for k in small], [grads[k] for k in small],
                                     [m_in[k] for k in small], [v_in[k] for k in small])
    deltas.update(zip(small, ds))
    new_m.update(zip(small, mns))
    new_v.update(zip(small, vns))

    return (loss, grad_x[None], *[grads[k] for k in names], *[deltas[k] for k in names],
            *[new_m[k] for k in names], *[new_v[k] for k in names])
```

```python
import functools
import math

import jax
import jax.numpy as jnp
import numpy as np
from jax import lax
from jax.experimental import pallas as pl
from jax.experimental.pallas import tpu as pltpu

F32 = jnp.float32
BF16 = jnp.bfloat16
MESH = pl.DeviceIdType.MESH

HEAD_DIM = 128
HEADS_PER_GROUP = 4
DILATED_PATTERNS = ((128, 1), (512, 4), (2048, 16))
N_GROUPS = 3
N_HEADS = N_GROUPS * HEADS_PER_GROUP
SPAN = 128
GROUP_WIDTH = HEADS_PER_GROUP * HEAD_DIM
CONV_WIDTH = 31
FFN_CONV_WIDTH = 3
N_BUCKETS = 32
MAX_DISTANCE = 2048
RMS_EPS = 1e-6
LN_EPS = 1e-5
NEG_INF = -1e30
ADAM_LR = 0.001
ADAM_B1 = 0.9
ADAM_B2 = 0.999
ADAM_EPS = 1e-08
ADAM_WD = 0.01
ADAM_STEP = 10

LANES = 128
SUBLANES = 8
PACKED_ROWS = 16
ROW_TILE = 512
GATE_ROWS, GATE_COLS = 512, 512
TIME_BLOCK = 128
CONV_PAD = 32
FFN_PAD = 8
VMEM_LIMIT = 56 << 20


def _params(sem=None, vmem=None):
    kw = {}
    if sem is not None:
        kw["dimension_semantics"] = sem
    if vmem is not None:
        kw["vmem_limit_bytes"] = vmem
    return pltpu.CompilerParams(**kw)


def _pick(n, cands):
    for c in cands:
        if n % c == 0:
            return c
    return n


ELEMENTWISE_TILE_BYTES = 3 << 19


def _row_tile(rows, cols):
    for align in (16, SUBLANES):
        fits = [t for t in range(align, rows + 1, align) if rows % t == 0 and t * cols * 4 <= ELEMENTWISE_TILE_BYTES]
        if fits:
            return max(fits)
    return SUBLANES


N_CHIPS = 4
M_TILES = (1024, 1408, 512, 256, 128)
N_TILES = (1024, 512, 1408, 256, 128)
K_TILES = (2176, 2048, 1408, 1024, 512, 256, 128)


def _matmul(a, b, mode, name, out_shards=False, tm=None):
    assert a.dtype == BF16 and b.dtype == BF16, (name, a.dtype, b.dtype)
    b3 = b.ndim == 3
    tn = tk = None
    halves = None
    if mode == "nn":
        M, K = a.shape
        N = b.shape[-1] * (N_CHIPS if b3 else 1)
        tn = b.shape[-1] if b3 else None
    elif mode == "nt":
        if a.ndim == 3:
            halves = a.shape[2]
        M, K = a.shape[-2], a.shape[-1] * (a.shape[0] if a.ndim == 3 else 1)
        N = b.shape[-2]
        tk = b.shape[-1] if b3 else None
    else:
        if b3:
            halves = b.shape[2]
        K, M = a.shape
        N = b.shape[-1] * (b.shape[0] if b3 else 1)
        tn = N // N_CHIPS if out_shards else None
    tm = tm or _pick(M, M_TILES)
    tn = tn or _pick(N, N_TILES)
    tk = tk or _pick(K, K_TILES)
    nk = K // tk
    dn = {"nn": (((1,), (0,)), ((), ())), "nt": (((1,), (1,)), ((), ())), "tn": (((0,), (0,)), ((), ()))}[mode]

    def body(a_ref, b_ref, o_ref):
        if nk == 1:
            o_ref[...] = lax.dot_general(a_ref[...], b_ref[...], dn, preferred_element_type=F32)
        else:
            @pl.when(pl.program_id(2) == 0)
            def _():
                o_ref[...] = jnp.zeros_like(o_ref)

            o_ref[...] += lax.dot_general(a_ref[...], b_ref[...], dn, preferred_element_type=F32)

    if mode == "tn":
        a_spec = pl.BlockSpec((tk, tm), lambda i, j, k: (k, i))
    elif halves:
        per = halves // tk
        a_spec = pl.BlockSpec((None, tm, tk), lambda i, j, k: (k // per, i, k % per))
    else:
        a_spec = pl.BlockSpec((tm, tk), lambda i, j, k: (i, k))
    if mode == "nn":
        b_spec = pl.BlockSpec((None, tk, tn), lambda i, j, k: (j, k, 0)) if b3 else pl.BlockSpec((tk, tn), lambda i, j, k: (k, j))
    elif mode == "nt":
        b_spec = pl.BlockSpec((None, tn, tk), lambda i, j, k: (k, j, 0)) if b3 else pl.BlockSpec((tn, tk), lambda i, j, k: (j, k))
    elif halves:
        per = halves // tn
        b_spec = pl.BlockSpec((None, tk, tn), lambda i, j, k: (j // per, k, j % per))
    else:
        b_spec = pl.BlockSpec((tk, tn), lambda i, j, k: (k, j))
    if out_shards:
        out_spec = pl.BlockSpec((None, tm, tn), lambda i, j, k: (j, i, 0))
        out_shape = jax.ShapeDtypeStruct((N_CHIPS, M, tn), F32)
    else:
        out_spec = pl.BlockSpec((tm, tn), lambda i, j, k: (i, j))
        out_shape = jax.ShapeDtypeStruct((M, N), F32)
    return pl.pallas_call(
        body, name=name, grid=(M // tm, N // tn, nk),
        in_specs=[a_spec, b_spec], out_specs=out_spec, out_shape=out_shape,
        compiler_params=_params(("parallel", "parallel", "arbitrary"), VMEM_LIMIT),
    )(a, b)


def _grad_half_matmul(a, name, chip_half, b, init=None):
    K, M = a.shape
    parts = b.ndim == 3
    N = b.shape[-1] * (b.shape[0] if parts else 1)
    h, tn = M // 2, N // N_CHIPS
    summed = init is not None
    dn = (((0,), (0,)), ((), ()))

    def body(ch_ref, a_ref, b_ref, *rest):
        product = lax.dot_general(a_ref[...], b_ref[...], dn, preferred_element_type=F32)
        if not summed:
            rest[0][...] = product
            return
        init_ref, own_ref, sum16_ref = rest
        total = product + init_ref[...]
        sum16_ref[...] = total.astype(BF16)

        @pl.when(pl.program_id(0) == ch_ref[0])
        def _():
            own_ref[...] = total

    if parts:
        per = b.shape[2] // tn
        b_spec = pl.BlockSpec((None, K, tn), lambda j, ch_ref: (j // per, 0, j % per))
    else:
        b_spec = pl.BlockSpec((K, tn), lambda j, ch_ref: (0, j))
    shard_spec = pl.BlockSpec((None, h, tn), lambda j, ch_ref: (j, 0, 0))
    own_spec = pl.BlockSpec((h, tn), lambda j, ch_ref: (0, 0))
    shape = (N_CHIPS, h, tn)
    return pl.pallas_call(
        body, name=name + ("_mine" if summed else "_theirs"),
        grid_spec=pltpu.PrefetchScalarGridSpec(
            num_scalar_prefetch=1, grid=(N_CHIPS,),
            in_specs=[pl.BlockSpec((K, h), lambda j, ch_ref: (0, ch_ref[1])), b_spec] + [shard_spec] * summed,
            out_specs=[own_spec, shard_spec] if summed else shard_spec),
        out_shape=[jax.ShapeDtypeStruct((h, tn), F32), jax.ShapeDtypeStruct(shape, BF16)] if summed
        else jax.ShapeDtypeStruct(shape, F32),
        compiler_params=_params(("arbitrary",), VMEM_LIMIT),
    )(chip_half, a, b, *([init] if summed else []))


def _rms(x, g):
    r = lax.rsqrt(jnp.mean(x * x, axis=-1, keepdims=True) + RMS_EPS)
    return x * r * g


def _rms_bwd(x, g, dy):
    r = lax.rsqrt(jnp.mean(x * x, axis=-1, keepdims=True) + RMS_EPS)
    n = x * r
    dn = dy * g
    dx = r * (dn - n * jnp.mean(dn * n, axis=-1, keepdims=True))
    return dx, jnp.sum(dy * n, axis=0, keepdims=True)


def _sigmoid(x):
    return 1.0 / (1.0 + jnp.exp(-x))


_GELU_C = math.sqrt(2.0 / math.pi)


def _gelu(x):
    return 0.5 * x * (1.0 + jnp.tanh(_GELU_C * (x + 0.044715 * x * x * x)))


def _gelu_and_grad(x):
    x2 = x * x
    t = jnp.tanh(_GELU_C * x * (1.0 + 0.044715 * x2))
    half = 0.5 * (1.0 + t)
    return x * half, half + (0.5 * _GELU_C) * x * (1.0 - t * t) * (1.0 + (3.0 * 0.044715) * x2)


def _row_spec(width, col_block=0):
    return pl.BlockSpec((ROW_TILE, width), lambda i: (i, col_block))


def _vec_spec(width, col_block=0):
    return pl.BlockSpec((1, width), lambda i: (0, col_block))


def _accumulate(ref, part):
    @pl.when(pl.program_id(0) == 0)
    def _():
        ref[...] = part

    @pl.when(pl.program_id(0) > 0)
    def _():
        ref[...] += part


def _rms_fwd_call(x, g):
    S, D = x.shape

    def body(x_ref, g_ref, h_ref):
        h_ref[...] = _rms(x_ref[...], g_ref[...]).astype(BF16)

    return pl.pallas_call(
        body, name="rms_mix_pre", grid=(S // ROW_TILE,),
        in_specs=[_row_spec(D), _vec_spec(D)], out_specs=_row_spec(D),
        out_shape=jax.ShapeDtypeStruct((S, D), BF16),
        compiler_params=_params(("parallel",)),
    )(x, g)


def _ln_silu_call(c1, g, b):
    S, C = c1.shape

    def body(c_ref, g_ref, b_ref, o_ref):
        xv = c_ref[...]
        mu = jnp.mean(xv, axis=-1, keepdims=True)
        xc = xv - mu
        var = jnp.mean(xc * xc, axis=-1, keepdims=True)
        z = xc * lax.rsqrt(var + LN_EPS) * g_ref[...] + b_ref[...]
        o_ref[...] = (z * _sigmoid(z)).astype(BF16)

    return pl.pallas_call(
        body, name="conv_ln_silu", grid=(S // ROW_TILE,),
        in_specs=[_row_spec(C), _vec_spec(C), _vec_spec(C)], out_specs=_row_spec(C),
        out_shape=jax.ShapeDtypeStruct((S, C), BF16),
        compiler_params=_params(("parallel",)),
    )(c1, g, b)


def _ln_silu_bwd_call(c1, g, b, dc):
    S, C = c1.shape

    def body(c_ref, g_ref, b_ref, dc_ref, dx_ref, dg_ref, db_ref):
        xv = c_ref[...]
        mu = jnp.mean(xv, axis=-1, keepdims=True)
        xc = xv - mu
        rs = lax.rsqrt(jnp.mean(xc * xc, axis=-1, keepdims=True) + LN_EPS)
        xh = xc * rs
        z = xh * g_ref[...] + b_ref[...]
        sg = _sigmoid(z)
        dz = dc_ref[...] * (sg * (1.0 + z * (1.0 - sg)))
        dxh = dz * g_ref[...]
        dx_ref[...] = rs * (dxh - jnp.mean(dxh, axis=-1, keepdims=True) - xh * jnp.mean(dxh * xh, axis=-1, keepdims=True))
        _accumulate(dg_ref, jnp.sum(dz * xh, axis=0, keepdims=True))
        _accumulate(db_ref, jnp.sum(dz, axis=0, keepdims=True))

    return pl.pallas_call(
        body, name="conv_ln_silu_bwd", grid=(S // ROW_TILE,),
        in_specs=[_row_spec(C), _vec_spec(C), _vec_spec(C), _row_spec(C)],
        out_specs=[_row_spec(C), _vec_spec(C), _vec_spec(C)],
        out_shape=[jax.ShapeDtypeStruct((S, C), F32), jax.ShapeDtypeStruct((1, C), F32), jax.ShapeDtypeStruct((1, C), F32)],
        compiler_params=_params(("arbitrary",)),
    )(c1, g, b, dc)


def _mix_call(proj, gate_col0, b_gate, y_a, y_c):
    S, D = y_a.shape
    w = GATE_COLS
    nc = D // w
    ga0, gc0 = gate_col0 // w, (gate_col0 + D) // w

    def body(ga_ref, gc_ref, ba_ref, bc_ref, ya_ref, yc_ref, o_ref):
        o_ref[...] = (_sigmoid(ga_ref[...] + ba_ref[...]) * ya_ref[...]
                      + _sigmoid(gc_ref[...] + bc_ref[...]) * yc_ref[...]).astype(BF16)

    tile = lambda off: pl.BlockSpec((GATE_ROWS, w), lambda i, j: (i, off + j))
    vec = lambda off: pl.BlockSpec((1, w), lambda i, j: (0, off + j))
    return pl.pallas_call(
        body, name="gate_mix", grid=(S // GATE_ROWS, nc),
        in_specs=[tile(ga0), tile(gc0), vec(0), vec(nc), tile(0), tile(0)],
        out_specs=tile(0), out_shape=jax.ShapeDtypeStruct((S, D), BF16),
        compiler_params=_params(("parallel", "parallel")),
    )(proj, proj, b_gate, b_gate, y_a, y_c)


def _window_stores(stage_ref, slot, dst_ref, rows, cols, sems):
    width = stage_ref.shape[-1]
    return [pltpu.make_async_copy(stage_ref.at[slot, p], dst_ref.at[rows, pl.ds(pl.multiple_of(c, LANES), width)], sems.at[slot, p])
            for p, c in enumerate(cols)]


def _staged_window_stores(stage_ref, dst_ref, sems, step, n_steps, rows, cols, fill):
    slot = step % 2
    copies = lambda s: _window_stores(stage_ref, s, dst_ref, rows, cols, sems)

    @pl.when(step >= 2)
    def _():
        for cp in copies(slot):
            cp.wait()

    fill(slot)
    for cp in copies(slot):
        cp.start()

    @pl.when(step == n_steps - 1)
    def _():
        for s in ([slot, 1 - slot] if n_steps > 1 else [slot]):
            for cp in copies(s):
                cp.wait()


def _mix_bwd_call(dmixed, proj, gate_col0, b_gate, y_a, y_c):
    S, D = y_a.shape
    w = GATE_COLS
    nc = D // w
    nr = S // GATE_ROWS
    ga0, gc0 = gate_col0 // w, (gate_col0 + D) // w

    def body(dm_ref, ga_ref, gc_ref, ba_ref, bc_ref, ya_ref, yc_ref, dya_ref, dyc_ref, dproj_ref, dba_ref, dbc_ref,
             stage_ref, sems):
        j, i = pl.program_id(0), pl.program_id(1)
        dm = dm_ref[...]
        sa = _sigmoid(ga_ref[...] + ba_ref[...])
        sc = _sigmoid(gc_ref[...] + bc_ref[...])
        dya_ref[...] = (dm * sa).astype(BF16)
        dyc_ref[...] = (dm * sc).astype(BF16)
        dga = dm * ya_ref[...] * sa * (1.0 - sa)
        dgc = dm * yc_ref[...] * sc * (1.0 - sc)

        def fill(slot):
            stage_ref[slot, 0] = dga.astype(BF16)
            stage_ref[slot, 1] = dgc.astype(BF16)

        rows = pl.ds(pl.multiple_of(i * GATE_ROWS, GATE_ROWS), GATE_ROWS)
        _staged_window_stores(stage_ref, dproj_ref, sems, j * nr + i, nc * nr, rows,
                              [gate_col0 + j * w, gate_col0 + D + j * w], fill)
        pa = jnp.sum(dga, axis=0, keepdims=True)
        pc = jnp.sum(dgc, axis=0, keepdims=True)

        @pl.when(i == 0)
        def _():
            dba_ref[...] = pa
            dbc_ref[...] = pc

        @pl.when(i > 0)
        def _():
            dba_ref[...] += pa
            dbc_ref[...] += pc

    tile = lambda off: pl.BlockSpec((GATE_ROWS, w), lambda j, i: (i, off + j))
    vec = lambda off: pl.BlockSpec((1, w), lambda j, i: (0, off + j))
    return pl.pallas_call(
        body, name="gate_mix_bwd", grid=(nc, nr),
        in_specs=[tile(0), tile(ga0), tile(gc0), vec(0), vec(nc), tile(0), tile(0)],
        out_specs=[tile(0), tile(0), ANY, vec(0), vec(0)],
        out_shape=[jax.ShapeDtypeStruct((S, D), BF16)] * 2 + [jax.ShapeDtypeStruct((S, proj.shape[1]), BF16)] + [
                   jax.ShapeDtypeStruct((1, D), F32), jax.ShapeDtypeStruct((1, D), F32)],
        scratch_shapes=[pltpu.VMEM((2, 2, GATE_ROWS, w), BF16), pltpu.SemaphoreType.DMA((2, 2))],
        compiler_params=_params(("arbitrary", "arbitrary")),
    )(dmixed, proj, proj, b_gate, b_gate, y_a, y_c)


def _res1_call(x, out, g_post, g_pre):
    S, D = x.shape

    def body(x_ref, o_ref, gp_ref, gq_ref, x1_ref, h2_ref):
        x1 = x_ref[...] + _rms(o_ref[...], gp_ref[...])
        x1_ref[...] = x1
        h2_ref[...] = _rms(x1, gq_ref[...]).astype(BF16)

    return pl.pallas_call(
        body, name="residual_mix", grid=(S // ROW_TILE,),
        in_specs=[_row_spec(D), _row_spec(D), _vec_spec(D), _vec_spec(D)],
        out_specs=[_row_spec(D), _row_spec(D)],
        out_shape=[jax.ShapeDtypeStruct((S, D), F32), jax.ShapeDtypeStruct((S, D), BF16)],
        compiler_params=_params(("parallel",)),
    )(x, out, g_post, g_pre)


def _loss_call(y, x1, g_post, target):
    S, D = y.shape

    def body(y_ref, x1_ref, g_ref, t_ref, loss_ref, dx_ref, dy_ref, dg_ref):
        yv, gv = y_ref[...], g_ref[...]
        err = x1_ref[...] + _rms(yv, gv) - t_ref[...]
        dx2 = err * (1.0 / D)
        dx_ref[...] = dx2
        dy, dg = _rms_bwd(yv, gv, dx2)
        dy_ref[...] = dy.astype(BF16)
        _accumulate(dg_ref, dg)
        part = 0.5 * jnp.sum(jnp.mean(err * err, axis=-1, keepdims=True), axis=0, keepdims=True)
        _accumulate(loss_ref, jnp.broadcast_to(part, (SUBLANES, LANES)))

    return pl.pallas_call(
        body, name="residual_ffn_loss", grid=(S // ROW_TILE,),
        in_specs=[_row_spec(D), _row_spec(D), _vec_spec(D), _row_spec(D)],
        out_specs=[pl.BlockSpec((SUBLANES, LANES), lambda i: (0, 0)), _row_spec(D), _row_spec(D), _vec_spec(D)],
        out_shape=[jax.ShapeDtypeStruct((SUBLANES, LANES), F32), jax.ShapeDtypeStruct((S, D), F32),
                   jax.ShapeDtypeStruct((S, D), BF16), jax.ShapeDtypeStruct((1, D), F32)],
        compiler_params=_params(("arbitrary",)),
    )(y, x1, g_post, target)


def _mid_bwd_call(x1, g_pre, dh2, dx2, out, g_post):
    S, D = x1.shape

    def body(x1_ref, gq_ref, dh_ref, dx2_ref, o_ref, gp_ref, dx1_ref, do_ref, dgq_ref, dgp_ref):
        d, dgq = _rms_bwd(x1_ref[...], gq_ref[...], dh_ref[...])
        dx1 = dx2_ref[...] + d
        dx1_ref[...] = dx1
        do, dgp = _rms_bwd(o_ref[...], gp_ref[...], dx1)
        do_ref[...] = do.astype(BF16)
        _accumulate(dgq_ref, dgq)
        _accumulate(dgp_ref, dgp)

    return pl.pallas_call(
        body, name="residual_mix_bwd", grid=(S // ROW_TILE,),
        in_specs=[_row_spec(D), _vec_spec(D), _row_spec(D), _row_spec(D), _row_spec(D), _vec_spec(D)],
        out_specs=[_row_spec(D), _row_spec(D), _vec_spec(D), _vec_spec(D)],
        out_shape=[jax.ShapeDtypeStruct((S, D), F32), jax.ShapeDtypeStruct((S, D), BF16)] + [jax.ShapeDtypeStruct((1, D), F32)] * 2,
        compiler_params=_params(("arbitrary",)),
    )(x1, g_pre, dh2, dx2, out, g_post)


def _in_bwd_call(x, g, dh1, dx1):
    S, D = x.shape

    def body(x_ref, g_ref, dh_ref, dx1_ref, gx_ref, dg_ref):
        d, dg = _rms_bwd(x_ref[...], g_ref[...], dh_ref[...])
        gx_ref[...] = dx1_ref[...] + d
        _accumulate(dg_ref, dg)

    return pl.pallas_call(
        body, name="rms_mix_pre_bwd", grid=(S // ROW_TILE,),
        in_specs=[_row_spec(D), _vec_spec(D), _row_spec(D), _row_spec(D)],
        out_specs=[_row_spec(D), _vec_spec(D)],
        out_shape=[jax.ShapeDtypeStruct((S, D), F32), jax.ShapeDtypeStruct((1, D), F32)],
        compiler_params=_params(("arbitrary",)),
    )(x, g, dh1, dx1)


def _bucket_table(dilation):
    qi = np.arange(SPAN)[:, None]
    ki = np.arange(2 * SPAN)[None, :]
    dist = np.maximum(qi + SPAN - ki, 0) * dilation
    max_exact = N_BUCKETS // 2
    d = np.maximum(dist, 1).astype(np.float64)
    large = max_exact + (np.log(d / max_exact) / math.log(MAX_DISTANCE / max_exact) * (N_BUCKETS - max_exact)).astype(np.int32)
    large = np.minimum(large, N_BUCKETS - 1)
    return np.where(dist < max_exact, dist, large).astype(np.int32)


def _bucket_tables():
    return jnp.asarray(np.stack([_bucket_table(r) for _, r in DILATED_PATTERNS]))


def _bias_table_call(rel_bias, buckets):
    def body(rb_ref, bk_ref, o_ref):
        for h in range(N_HEADS):
            bk = bk_ref[h // HEADS_PER_GROUP]

            def step(b, acc):
                return jnp.where(bk == b, rb_ref[b, h], acc)

            o_ref[h] = lax.fori_loop(0, N_BUCKETS, step, jnp.zeros((SPAN, 2 * SPAN), F32))

    return pl.pallas_call(
        body, name="rel_bias_table",
        in_specs=[pl.BlockSpec(memory_space=pltpu.SMEM), pl.BlockSpec(memory_space=pltpu.VMEM)],
        out_specs=pl.BlockSpec(memory_space=pltpu.VMEM),
        out_shape=jax.ShapeDtypeStruct((N_HEADS, SPAN, 2 * SPAN), F32),
    )(rel_bias, buckets)


def _bias_grad_call(dbias, buckets):
    def body(db_ref, bk_ref, o_ref, rows_ref):
        for h in range(N_HEADS):
            bk = bk_ref[h // HEADS_PER_GROUP]
            dv = db_ref[h]

            def step(b, carry):
                rows_ref[h, b] = jnp.sum(jnp.where(bk == b, dv, 0.0), axis=0, keepdims=True)
                return carry

            lax.fori_loop(0, N_BUCKETS, step, 0)
        o_ref[...] = jnp.sum(rows_ref[...], axis=-1, keepdims=True)

    out = pl.pallas_call(
        body, name="rel_bias_grad",
        in_specs=[pl.BlockSpec(memory_space=pltpu.VMEM), pl.BlockSpec(memory_space=pltpu.VMEM)],
        out_specs=pl.BlockSpec(memory_space=pltpu.VMEM),
        out_shape=jax.ShapeDtypeStruct((N_HEADS, N_BUCKETS, 1, 1), F32),
        scratch_shapes=[pltpu.VMEM((N_HEADS, N_BUCKETS, 1, 2 * SPAN), F32)],
    )(dbias, buckets)
    return out.reshape(N_HEADS, N_BUCKETS).T


def _dot_nt(a, b):
    return lax.dot_general(a, b, (((1,), (1,)), ((), ())), preferred_element_type=F32)


def _dot_nn(a, b):
    return lax.dot_general(a, b, (((1,), (0,)), ((), ())), preferred_element_type=F32)


def _dot_tn(a, b):
    return lax.dot_general(a, b, (((0,), (0,)), ((), ())), preferred_element_type=F32)


def _band_masks(n, nb):
    qi = lax.broadcasted_iota(jnp.int32, (SPAN, SPAN), 0)
    ki = lax.broadcasted_iota(jnp.int32, (SPAN, SPAN), 1)
    prev_ok = jnp.logical_and(ki >= qi, n > 0)
    cur_ok = ki <= qi
    next_ok = jnp.logical_and(ki >= qi, n < nb - 1)
    return prev_ok, cur_ok, next_ok


def _wide_band_mask(n):
    qi = lax.broadcasted_iota(jnp.int32, (SPAN, 2 * SPAN), 0)
    ki = lax.broadcasted_iota(jnp.int32, (SPAN, 2 * SPAN), 1)
    prev_ok = jnp.logical_and(jnp.logical_and(ki < SPAN, ki >= qi), n > 0)
    cur_ok = jnp.logical_and(ki >= SPAN, ki - SPAN <= qi)
    return jnp.logical_or(prev_ok, cur_ok)


def _attn_plan(S, group):
    r = DILATED_PATTERNS[group][1]
    hp, per = (HEADS_PER_GROUP, 1) if r == 1 else (2, 4)
    return r, S // (r * SPAN), hp, per


def _residue_rows(rho, r):
    return slice(None) if r == 1 else pl.ds(rho, SPAN, stride=r)


def _for_residues(r, per, fn):
    if r == per:
        for u in range(per):
            fn(u)
        return

    def step(i, carry):
        for u in range(per):
            fn(i * per + u)
        return carry

    lax.fori_loop(0, r // per, step, 0)


def _attn_fwd_call(proj, bias, group):
    S = proj.shape[0]
    r, nb, hp, per = _attn_plan(S, group)
    scale = HEAD_DIM ** -0.5
    kinds = ("q", "kp", "kc", "vp", "vc") if nb > 1 else ("q", "kc", "vc")

    per_kind = _refs_per_kind(r, hp)

    def body(*refs):
        ins = {kind: refs[i * per_kind:(i + 1) * per_kind] for i, kind in enumerate(kinds)}
        b_ref, o_ref, lse_ref = refs[len(kinds) * per_kind:]
        n = pl.program_id(1)
        prev_ok, cur_ok, _ = _band_masks(n, nb)

        band_ok = _wide_band_mask(n) if nb > 1 else cur_ok

        def residue(rho):
            rows = _residue_rows(rho, r)
            for j in range(hp):
                get = lambda kind: _head_rows(ins[kind], j, rows, r).astype(BF16)
                q = get("q")
                if nb > 1:
                    keys, vals, bias_j = jnp.concatenate([get("kp"), get("kc")], axis=0), jnp.concatenate([get("vp"), get("vc")], axis=0), b_ref[j]
                else:
                    keys, vals, bias_j = get("kc"), get("vc"), b_ref[j, :, SPAN:]
                s = jnp.where(band_ok, _dot_nt(q, keys) * scale + bias_j, NEG_INF)
                m = jnp.max(s, axis=-1, keepdims=True)
                p = jnp.exp(s - m)
                den = jnp.sum(p, axis=-1, keepdims=True)
                o_ref[j, rows, :] = _dot_nn(p.astype(BF16), vals) / den
                lse_ref[j, rows, :] = jnp.broadcast_to(m + jnp.log(den), (SPAN, HEAD_DIM))

        _for_residues(r, per, residue)

    in_specs = [_head_spec(r, nb, hp, kind, group, jj) for kind in kinds for jj in range(per_kind)]
    in_specs.append(pl.BlockSpec((hp, SPAN, 2 * SPAN), lambda j, n: (group * (HEADS_PER_GROUP // hp) + j, 0, 0)))
    out = pl.BlockSpec((hp, r * SPAN, HEAD_DIM), lambda j, n: (j, n, 0))
    return pl.pallas_call(
        body, name=f"attn_fwd_g{group}", grid=(HEADS_PER_GROUP // hp, nb),
        in_specs=in_specs, out_specs=[out] * 2,
        out_shape=[jax.ShapeDtypeStruct((HEADS_PER_GROUP, S, HEAD_DIM), F32)] * 2,
        compiler_params=_params(("parallel", "parallel"), VMEM_LIMIT),
    )(*([proj] * (len(in_specs) - 1)), bias)


_PROJ_PART = dict(q=0, qn=0, kp=1, kc=1, vp=2, vc=2)


def _refs_per_kind(r, hp):
    return 1 if r == 1 else hp


def _head_rows(refs, j, rows, r):
    return refs[0][:, j * HEAD_DIM:(j + 1) * HEAD_DIM] if r == 1 else refs[j][rows, :]


def _head_spec(r, nb, hp, kind, group, jj):
    if kind in _PROJ_PART:
        base = (_PROJ_PART[kind] * N_GROUPS + group) * HEADS_PER_GROUP
    else:
        base = 0
    if kind.endswith("p"):
        row = lambda n: jnp.maximum(n - 1, 0)
    elif kind.endswith("n"):
        row = lambda n: jnp.minimum(n + 1, nb - 1)
    else:
        row = lambda n: n
    if r == 1:
        return pl.BlockSpec((SPAN, hp * HEAD_DIM), lambda j, n: (row(n), base // hp + j))
    return pl.BlockSpec((r * SPAN, HEAD_DIM), lambda j, n: (row(n), base + j * hp + jj))


def _attn_merge_call(parts):
    S = parts[0].shape[1]

    def body(o1, s1, o2, s2, o3, s3, a_ref, ab_ref, lse_ref):
        for j in range(HEADS_PER_GROUP):
            sl = slice(j * HEAD_DIM, (j + 1) * HEAD_DIM)
            mx = jnp.maximum(jnp.maximum(s1[j], s2[j]), s3[j])
            w1 = jnp.exp(s1[j] - mx)
            w2 = jnp.exp(s2[j] - mx)
            w3 = jnp.exp(s3[j] - mx)
            den = w1 + w2 + w3
            a = (w1 * o1[j] + w2 * o2[j] + w3 * o3[j]) / den
            a_ref[:, sl] = a
            ab_ref[:, sl] = a.astype(BF16)
            lse_ref[:, sl] = mx + jnp.log(den)

    heads = pl.BlockSpec((HEADS_PER_GROUP, ROW_TILE, HEAD_DIM), lambda i: (0, i, 0))
    return pl.pallas_call(
        body, name="attn_merge", grid=(S // ROW_TILE,),
        in_specs=[heads] * 6, out_specs=[_row_spec(GROUP_WIDTH)] * 3,
        out_shape=[jax.ShapeDtypeStruct((S, GROUP_WIDTH), F32), jax.ShapeDtypeStruct((S, GROUP_WIDTH), BF16),
                   jax.ShapeDtypeStruct((S, GROUP_WIDTH), F32)],
        compiler_params=_params(("parallel",)),
    )(*parts)


def _attn_delta_call(a, da):
    S = a.shape[0]

    def body(a_ref, da_ref, d_ref):
        for j in range(HEADS_PER_GROUP):
            sl = slice(j * HEAD_DIM, (j + 1) * HEAD_DIM)
            d = jnp.sum(a_ref[:, sl] * da_ref[:, sl], axis=-1, keepdims=True)
            d_ref[:, sl] = jnp.broadcast_to(d, (ROW_TILE, HEAD_DIM))

    return pl.pallas_call(
        body, name="attn_delta", grid=(S // ROW_TILE,),
        in_specs=[_row_spec(GROUP_WIDTH)] * 2, out_specs=_row_spec(GROUP_WIDTH),
        out_shape=jax.ShapeDtypeStruct((S, GROUP_WIDTH), F32),
        compiler_params=_params(("parallel",)),
    )(a, da)


def _attn_bwd_call(proj, bias, da, lse, delta, group, dproj):
    S = proj.shape[0]
    r, nb, hp, per = _attn_plan(S, group)
    scale = HEAD_DIM ** -0.5
    kinds = ("q", "qn", "kp", "kc", "vp", "vc", "da", "dan", "lse", "lsen", "dl", "dln") if nb > 1 else ("q", "kc", "vc", "da", "lse", "dl")
    source = dict(da=da, dan=da, lse=lse, lsen=lse, dl=delta, dln=delta)

    per_kind = _refs_per_kind(r, hp)
    per_group = HEADS_PER_GROUP // hp
    block_rows = r * SPAN

    def body(*refs):
        ins = {kind: refs[i * per_kind:(i + 1) * per_kind] for i, kind in enumerate(kinds)}
        b_ref, _, dproj_ref, db_ref, stage_ref, sems = refs[len(kinds) * per_kind:][:6]
        strided_ref = None if r == 1 else refs[-1]
        jg, n = pl.program_id(0), pl.program_id(1)
        prev_ok, cur_ok, next_ok = _band_masks(n, nb)

        @pl.when(n == 0)
        def _():
            db_ref[...] = jnp.zeros_like(db_ref)

        band_ok = _wide_band_mask(n) if nb > 1 else cur_ok

        def fill(slot):
            def put(part, j, rows, value):
                if r == 1:
                    stage_ref[slot, part, :, j * HEAD_DIM:(j + 1) * HEAD_DIM] = value.astype(BF16)
                else:
                    strided_ref[part, j, rows, :] = value

            _for_residues(r, per, functools.partial(residue, put))
            if r > 1:
                for part in range(3):
                    for j in range(hp):
                        for t0 in range(0, block_rows, ROW_TILE):
                            stage_ref[slot, part, t0:t0 + ROW_TILE, j * HEAD_DIM:(j + 1) * HEAD_DIM] = (
                                strided_ref[part, j, t0:t0 + ROW_TILE, :].astype(BF16))

        def residue(put, rho):
            rows = _residue_rows(rho, r)
            for j in range(hp):
                get = lambda kind: _head_rows(ins[kind], j, rows, r)
                q = get("q").astype(BF16)
                kc = get("kc").astype(BF16)
                vc = get("vc").astype(BF16)
                dav = get("da").astype(BF16)
                lse_q, dl_q = get("lse"), get("dl")
                if nb == 1:
                    pc = jnp.exp(jnp.where(cur_ok, _dot_nt(q, kc) * scale + b_ref[j, :, SPAN:], NEG_INF) - lse_q)
                    dsc = pc * (_dot_nt(dav, vc) - dl_q)
                    dsc_b = dsc.astype(BF16)
                    dq = _dot_nn(dsc_b, kc)
                    dk = _dot_tn(dsc_b, q)
                    dv = _dot_tn(pc.astype(BF16), dav)
                    db_ref[j, :, SPAN:] += dsc
                else:
                    qn = get("qn").astype(BF16)
                    dan = get("dan").astype(BF16)
                    keys = jnp.concatenate([get("kp").astype(BF16), kc], axis=0)
                    vals = jnp.concatenate([get("vp").astype(BF16), vc], axis=0)
                    wide = lambda t: jnp.concatenate([t, t], axis=1)
                    p = jnp.exp(jnp.where(band_ok, _dot_nt(q, keys) * scale + b_ref[j], NEG_INF) - wide(lse_q))
                    ds = p * (_dot_nt(dav, vals) - wide(dl_q))
                    dq = _dot_nn(ds.astype(BF16), keys)
                    db_ref[j] += ds
                    pn = jnp.exp(jnp.where(next_ok, _dot_nt(qn, kc) * scale + b_ref[j, :, :SPAN], NEG_INF) - get("lsen"))
                    dsn = pn * (_dot_nt(dan, vc) - get("dln"))
                    both = lambda cur_part, next_part: jnp.concatenate([cur_part.astype(BF16), next_part.astype(BF16)], axis=0)
                    dk = _dot_tn(both(ds[:, SPAN:], dsn), jnp.concatenate([q, qn], axis=0))
                    dv = _dot_tn(both(p[:, SPAN:], pn), jnp.concatenate([dav, dan], axis=0))
                put(0, j, rows, dq * scale)
                put(1, j, rows, dk * scale)
                put(2, j, rows, dv)

        cols = [(part * N_GROUPS + group) * GROUP_WIDTH + jg * (hp * HEAD_DIM) for part in range(3)]
        rows = pl.ds(pl.multiple_of(n * block_rows, SPAN), block_rows)
        _staged_window_stores(stage_ref, dproj_ref, sems, jg * nb + n, per_group * nb, rows, cols, fill)

    band = (hp, SPAN, 2 * SPAN)
    in_specs = [_head_spec(r, nb, hp, kind, group, jj) for kind in kinds for jj in range(per_kind)]
    in_specs += [pl.BlockSpec(band, lambda j, n: (group * per_group + j, 0, 0)), ANY]
    operands = [source.get(kind, proj) for kind in kinds for _ in range(per_kind)] + [bias, dproj]
    scratch = [pltpu.VMEM((2, 3, block_rows, hp * HEAD_DIM), BF16), pltpu.SemaphoreType.DMA((2, 3))]
    if r > 1:
        scratch.append(pltpu.VMEM((3, hp, block_rows, HEAD_DIM), F32))
    return pl.pallas_call(
        body, name=f"attn_bwd_g{group}", grid=(per_group, nb),
        in_specs=in_specs,
        out_specs=[ANY, pl.BlockSpec(band, lambda j, n: (j, 0, 0))],
        out_shape=[jax.ShapeDtypeStruct(dproj.shape, BF16), jax.ShapeDtypeStruct((HEADS_PER_GROUP, SPAN, 2 * SPAN), F32)],
        input_output_aliases={len(operands) - 1: 0},
        scratch_shapes=scratch,
        compiler_params=_params(("arbitrary", "arbitrary"), VMEM_LIMIT),
    )(*operands)


def _tap_rows(xpad_ref, t0, k, width, pad):
    return xpad_ref[pl.ds(t0 + (pad - (width - 1 - k)), TIME_BLOCK), :]


def _conv_block(xpad_ref, t0, w_ref, width, pad):
    acc = None
    for k in range(width):
        term = w_ref[k:k + 1, :] * _tap_rows(xpad_ref, t0, k, width, pad)
        acc = term if acc is None else acc + term
    return acc


def _conv_transpose_block(dpad_ref, t0, w_ref, width):
    acc = None
    for k in range(width):
        term = w_ref[k:k + 1, :] * dpad_ref[pl.ds(t0 + (width - 1 - k), TIME_BLOCK), :]
        acc = term if acc is None else acc + term
    return acc


def _conv_weight_grad(xpad_ref, t0, dy, dw_ref, width, pad):
    for k in range(width):
        dw_ref[k:k + 1, :] += jnp.sum(dy * _tap_rows(xpad_ref, t0, k, width, pad), axis=0, keepdims=True)


def _time_loop(S, step, skip_first=0, skip_last=0):
    def it(tb, carry):
        step(pl.multiple_of(tb * TIME_BLOCK, TIME_BLOCK))
        return carry

    lax.fori_loop(skip_first, S // TIME_BLOCK - skip_last, it, 0)


def _fill_head(head_ref, x_ref, pad):
    head_ref[0:pad, :] = jnp.zeros((pad, LANES), F32)
    head_ref[pad:, :] = x_ref[0:TIME_BLOCK, :]


def _fill_tail(tail_ref, x_ref, pad):
    S = x_ref.shape[0]
    tail_ref[0:TIME_BLOCK, :] = x_ref[S - TIME_BLOCK:S, :]
    tail_ref[TIME_BLOCK:, :] = jnp.zeros((pad, LANES), F32)


def _conv_fwd_call(proj, col0, w, b):
    S = proj.shape[0]
    C = w.shape[1]
    nt = C // LANES
    v0, g0 = col0 // LANES, (col0 + C) // LANES

    def body(val_ref, gate_ref, w_ref, b_ref, o_ref, pad_ref):
        pad_ref[0:CONV_PAD, :] = jnp.zeros((CONV_PAD, LANES), F32)
        pad_ref[CONV_PAD:, :] = val_ref[...] * _sigmoid(gate_ref[...])

        def step(t0):
            o_ref[pl.ds(t0, TIME_BLOCK), :] = _conv_block(pad_ref, t0, w_ref, CONV_WIDTH, CONV_PAD) + b_ref[...]

        _time_loop(S, step)

    seq = lambda off: pl.BlockSpec((S, LANES), lambda i: (0, off + i))
    return pl.pallas_call(
        body, name="conv_module", grid=(nt,),
        in_specs=[seq(v0), seq(g0), pl.BlockSpec((CONV_WIDTH, LANES), lambda i: (0, i)), pl.BlockSpec((1, LANES), lambda i: (0, i))],
        out_specs=seq(0), out_shape=jax.ShapeDtypeStruct((S, C), F32),
        scratch_shapes=[pltpu.VMEM((S + CONV_PAD, LANES), F32)],
        compiler_params=_params(("parallel",)),
    )(proj, proj, w, b)


def _conv_bwd_call(proj, col0, w, dc1, dproj):
    S = proj.shape[0]
    C = w.shape[1]
    nt = C // LANES
    v0, g0 = col0 // LANES, (col0 + C) // LANES

    def body(val_ref, gate_ref, w_ref, dy_ref, _, dproj_ref, dw_ref, db_ref, xpad_ref, tail_ref, dwacc_ref, stage_ref, sems):
        i = pl.program_id(0)
        xpad_ref[0:CONV_PAD, :] = jnp.zeros((CONV_PAD, LANES), F32)
        xpad_ref[CONV_PAD:, :] = val_ref[...] * _sigmoid(gate_ref[...])
        _fill_tail(tail_ref, dy_ref, CONV_PAD)
        dwacc_ref[...] = jnp.zeros_like(dwacc_ref)

        def fill(slot):
            def block(t0, dy_src, dy_t0):
                rows = pl.ds(t0, TIME_BLOCK)
                _conv_weight_grad(xpad_ref, t0, dy_ref[rows, :], dwacc_ref, CONV_WIDTH, CONV_PAD)
                dc0 = _conv_transpose_block(dy_src, dy_t0, w_ref, CONV_WIDTH)
                sg = _sigmoid(gate_ref[rows, :])
                stage_ref[slot, 0, rows, :] = (dc0 * sg).astype(BF16)
                stage_ref[slot, 1, rows, :] = (dc0 * val_ref[rows, :] * sg * (1.0 - sg)).astype(BF16)

            _time_loop(S, lambda t0: block(t0, dy_ref, t0), skip_last=1)
            block(S - TIME_BLOCK, tail_ref, 0)

        _staged_window_stores(stage_ref, dproj_ref, sems, i, nt, pl.ds(0, S),
                              [col0 + i * LANES, col0 + C + i * LANES], fill)
        dw_ref[...] = dwacc_ref[...]
        db_ref[...] = jnp.sum(dy_ref[...], axis=0, keepdims=True)

    seq = lambda off: pl.BlockSpec((S, LANES), lambda i: (0, off + i))
    return pl.pallas_call(
        body, name="conv_module_bwd", grid=(nt,),
        in_specs=[seq(v0), seq(g0), pl.BlockSpec((CONV_WIDTH, LANES), lambda i: (0, i)), seq(0), ANY],
        out_specs=[ANY, pl.BlockSpec((CONV_PAD, LANES), lambda i: (0, i)), pl.BlockSpec((1, LANES), lambda i: (0, i))],
        out_shape=[jax.ShapeDtypeStruct(dproj.shape, BF16),
                   jax.ShapeDtypeStruct((CONV_PAD, C), F32), jax.ShapeDtypeStruct((1, C), F32)],
        input_output_aliases={4: 0},
        scratch_shapes=[pltpu.VMEM((S + CONV_PAD, LANES), F32), pltpu.VMEM((TIME_BLOCK + CONV_PAD, LANES), F32),
                        pltpu.VMEM((CONV_PAD, LANES), F32),
                        pltpu.VMEM((2, 2, S, LANES), BF16), pltpu.SemaphoreType.DMA((2, 2))],
        compiler_params=_params(("arbitrary",)),
    )(proj, proj, w, dc1, dproj)


def _ffn_fwd_call(u, w, b):
    S, C2 = u.shape
    C = C2 // 2
    nt = C // LANES

    def body(ug_ref, uv_ref, wg_ref, wv_ref, bg_ref, bv_ref, f_ref, hg_ref, hv_ref):
        _fill_head(hg_ref, ug_ref, FFN_PAD)
        _fill_head(hv_ref, uv_ref, FFN_PAD)

        def block(t0, xg_ref, xv_ref, x_t0, pad):
            cg = _conv_block(xg_ref, x_t0, wg_ref, FFN_CONV_WIDTH, pad) + bg_ref[...]
            cv = _conv_block(xv_ref, x_t0, wv_ref, FFN_CONV_WIDTH, pad) + bv_ref[...]
            f_ref[pl.ds(t0, TIME_BLOCK), :] = (_gelu(cg) * cv).astype(BF16)

        block(0, hg_ref, hv_ref, 0, FFN_PAD)
        _time_loop(S, lambda t0: block(t0, ug_ref, uv_ref, t0, 0), skip_first=1)

    seq = lambda off: pl.BlockSpec((S, LANES), lambda i: (0, off + i))
    wsp = lambda off: pl.BlockSpec((FFN_CONV_WIDTH, LANES), lambda i: (0, off + i))
    bsp = lambda off: pl.BlockSpec((1, LANES), lambda i: (0, off + i))
    return pl.pallas_call(
        body, name="ffn_conv_geglu", grid=(nt,),
        in_specs=[seq(0), seq(nt), wsp(0), wsp(nt), bsp(0), bsp(nt)],
        out_specs=seq(0), out_shape=jax.ShapeDtypeStruct((S, C), BF16),
        scratch_shapes=[pltpu.VMEM((FFN_PAD + TIME_BLOCK, LANES), F32)] * 2,
        compiler_params=_params(("parallel",)),
    )(u, u, w, w, b, b)


def _ffn_bwd_call(u, w, b, df):
    S, C2 = u.shape
    C = C2 // 2
    nt = C // LANES

    def body(ug_ref, uv_ref, wg_ref, wv_ref, bg_ref, bv_ref, df_ref,
             du_ref, dwg_ref, dwv_ref, dbg_ref, dbv_ref,
             hg_ref, hv_ref, dg_ref, dv_ref, dwg_acc, dwv_acc, dbg_acc, dbv_acc):
        zeros = jnp.zeros((FFN_PAD, LANES), F32)
        _fill_head(hg_ref, ug_ref, FFN_PAD)
        _fill_head(hv_ref, uv_ref, FFN_PAD)
        dg_ref[S:, :] = zeros
        dv_ref[S:, :] = zeros
        dwg_acc[...] = jnp.zeros_like(dwg_acc)
        dwv_acc[...] = jnp.zeros_like(dwv_acc)
        dbg_acc[...] = jnp.zeros_like(dbg_acc)
        dbv_acc[...] = jnp.zeros_like(dbv_acc)

        def first(t0, xg_ref, xv_ref, x_t0, pad):
            rows = pl.ds(t0, TIME_BLOCK)
            cg = _conv_block(xg_ref, x_t0, wg_ref, FFN_CONV_WIDTH, pad) + bg_ref[...]
            cv = _conv_block(xv_ref, x_t0, wv_ref, FFN_CONV_WIDTH, pad) + bv_ref[...]
            dfb = df_ref[rows, :]
            gelu, gelu_grad = _gelu_and_grad(cg)
            dcg = dfb * cv * gelu_grad
            dcv = dfb * gelu
            dg_ref[rows, :] = dcg
            dv_ref[rows, :] = dcv
            _conv_weight_grad(xg_ref, x_t0, dcg, dwg_acc, FFN_CONV_WIDTH, pad)
            _conv_weight_grad(xv_ref, x_t0, dcv, dwv_acc, FFN_CONV_WIDTH, pad)
            dbg_acc[...] += jnp.sum(dcg, axis=0, keepdims=True)
            dbv_acc[...] += jnp.sum(dcv, axis=0, keepdims=True)

        def second(t0):
            rows = pl.ds(t0, TIME_BLOCK)
            du_ref[0, rows, :] = _conv_transpose_block(dg_ref, t0, wg_ref, FFN_CONV_WIDTH).astype(BF16)
            du_ref[1, rows, :] = _conv_transpose_block(dv_ref, t0, wv_ref, FFN_CONV_WIDTH).astype(BF16)

        first(0, hg_ref, hv_ref, 0, FFN_PAD)
        _time_loop(S, lambda t0: first(t0, ug_ref, uv_ref, t0, 0), skip_first=1)
        _time_loop(S, second)
        dwg_ref[...] = dwg_acc[...]
        dwv_ref[...] = dwv_acc[...]
        dbg_ref[...] = dbg_acc[...]
        dbv_ref[...] = dbv_acc[...]

    seq = lambda off: pl.BlockSpec((S, LANES), lambda i: (0, off + i))
    wsp = lambda off: pl.BlockSpec((FFN_CONV_WIDTH, LANES), lambda i: (0, off + i))
    bsp = lambda off: pl.BlockSpec((1, LANES), lambda i: (0, off + i))
    return pl.pallas_call(
        body, name="ffn_conv_geglu_bwd", grid=(nt,),
        in_specs=[seq(0), seq(nt), wsp(0), wsp(nt), bsp(0), bsp(nt), seq(0)],
        out_specs=[pl.BlockSpec((2, S, LANES), lambda i: (0, 0, i)),
                   pl.BlockSpec((SUBLANES, LANES), lambda i: (0, i)), pl.BlockSpec((SUBLANES, LANES), lambda i: (0, i)),
                   bsp(0), bsp(0)],
        out_shape=[jax.ShapeDtypeStruct((2, S, C), BF16)] + [jax.ShapeDtypeStruct((SUBLANES, C), F32)] * 2
        + [jax.ShapeDtypeStruct((1, C), F32)] * 2,
        scratch_shapes=[pltpu.VMEM((FFN_PAD + TIME_BLOCK, LANES), F32)] * 2 + [pltpu.VMEM((S + FFN_PAD, LANES), F32)] * 2
        + [pltpu.VMEM((SUBLANES, LANES), F32)] * 2
        + [pltpu.VMEM((1, LANES), F32)] * 2,
        compiler_params=_params(("parallel",)),
    )(u, u, w, w, b, b, df)


def _adamw(w_ref, g_ref, m_ref, v_ref, d_ref, mo_ref, vo_ref):
    gv = g_ref[...]
    mn = ADAM_B1 * m_ref[...] + (1.0 - ADAM_B1) * gv
    vn = ADAM_B2 * v_ref[...] + (1.0 - ADAM_B2) * (gv * gv)
    mo_ref[...] = mn
    vo_ref[...] = vn
    m_hat = mn * (1.0 / (1.0 - ADAM_B1 ** ADAM_STEP))
    v_hat = vn * (1.0 / (1.0 - ADAM_B2 ** ADAM_STEP))
    d_ref[...] = -ADAM_LR * (m_hat / (jnp.sqrt(v_hat) + ADAM_EPS) + ADAM_WD * w_ref[...])


def _adamw_call(w, g, m, v, name):
    R, C = w.shape
    tr = _row_tile(R, C)

    def body(w_ref, g_ref, m_ref, v_ref, go_ref, d_ref, mo_ref, vo_ref):
        go_ref[...] = g_ref[...]
        _adamw(w_ref, g_ref, m_ref, v_ref, d_ref, mo_ref, vo_ref)

    spec = pl.BlockSpec((tr, C), lambda i: (i, 0))
    return pl.pallas_call(
        body, name=name, grid=(R // tr,),
        in_specs=[spec] * 4, out_specs=[spec] * 4,
        out_shape=[jax.ShapeDtypeStruct((R, C), F32)] * 4,
        compiler_params=_params(("parallel",)),
    )(w, g, m, v)


def _adamw_small_call(ws, gs, ms, vs):
    n = len(ws)

    def body(*refs):
        w_refs, g_refs, m_refs, v_refs, d_refs, mo_refs, vo_refs = (refs[i * n:(i + 1) * n] for i in range(7))
        for i in range(n):
            _adamw(w_refs[i], g_refs[i], m_refs[i], v_refs[i], d_refs[i], mo_refs[i], vo_refs[i])

    whole = pl.BlockSpec(memory_space=pltpu.VMEM)
    outs = pl.pallas_call(
        body, name="adamw_small",
        in_specs=[whole] * (4 * n), out_specs=[whole] * (3 * n),
        out_shape=[jax.ShapeDtypeStruct(w.shape, F32) for w in ws] * 3,
    )(*ws, *gs, *ms, *vs)
    return outs[:n], outs[n:2 * n], outs[2 * n:]


def _position():
    return lax.axis_index("x"), lax.axis_index("y"), lax.axis_index("c")


def _chip_peers(x, y):
    return [(x, 1 - y), (1 - x, y), (1 - x, 1 - y)]


def _half_rows(ref, core, rows):
    h = rows // 2
    start = pl.multiple_of(core * h, PACKED_ROWS)
    return ref.at[pl.ds(start, h), :] if len(ref.shape) == 2 else ref.at[:, pl.ds(start, h), :]


def _shard_half(ref, shard, core, rows):
    h = rows // 2
    return ref.at[shard, pl.ds(pl.multiple_of(core * h, PACKED_ROWS), h), :]


ANY = pl.BlockSpec(memory_space=pl.ANY)


def _first_hop_copies(srcs, lands):
    x, y, c = _position()
    chip = 2 * x + y
    targets = [(px, py, c) for px, py in _chip_peers(x, y)] + [(x, y, 1 - c)]
    rows = srcs[0].shape[0]
    out = []
    for i, (s, l) in enumerate(zip(srcs, lands)):
        for k, dev in enumerate(targets):
            if i == 0 and k < 3:
                out.append((_half_rows(s, c, rows), _shard_half(l, chip, c, rows), dev, k))
            else:
                out.append((s, l.at[chip], dev, len(targets) * i + k))
    return out


def _second_hop_copies(srcs, lands):
    x, y, c = _position()
    rows = lands[0].shape[1]
    out = []
    for k, (px, py) in enumerate(_chip_peers(x, y)):
        half = _shard_half(lands[0], 2 * px + py, c, rows)
        out.append((half, half, (x, y, 1 - c), k))
    return out


HBM_SPEC = pl.BlockSpec(memory_space=pltpu.HBM)
SEM_SPEC = pl.BlockSpec(memory_space=pltpu.SEMAPHORE)
DATAFLOW = pltpu.SideEffectType.DATAFLOW_SIDE_EFFECTING


def _in_hbm(a):
    return pltpu.with_memory_space_constraint(a, pltpu.HBM)


def _split_start(name, groups, after, carry=None):
    spans, arrays = [], []
    for srcs, lands, _, _ in groups:
        spans.append((len(arrays), len(srcs), len(lands)))
        arrays += list(srcs) + list(lands)
    if carry is not None:
        arrays.append(carry)
    na, ng = len(arrays), len(groups)

    def body(*refs):
        sems, token = refs[na + 1:na + 1 + 2 * ng], refs[-1]
        for g, (_, _, _, copies) in enumerate(groups):
            off, ns, nl = spans[g]
            for src, dst, dev, idx in copies(refs[off:off + ns], refs[off + ns:off + ns + nl]):
                pltpu.make_async_remote_copy(src_ref=src, dst_ref=dst, send_sem=sems[2 * g].at[idx], recv_sem=sems[2 * g + 1].at[idx],
                                             device_id=dev, device_id_type=MESH).start()
        token[...] = jnp.zeros_like(token)

    outs = pl.pallas_call(
        body, name=name,
        in_specs=[HBM_SPEC] * na + [ANY],
        out_specs=[SEM_SPEC] * (2 * ng) + [HBM_SPEC] * na + [pl.BlockSpec(memory_space=pltpu.VMEM)],
        out_shape=[pltpu.SemaphoreType.DMA((n_sems,)) for _, _, n_sems, _ in groups for _ in range(2)]
        + [pltpu.HBM(a.shape, a.dtype) for a in arrays] + [jax.ShapeDtypeStruct((SUBLANES, LANES), F32)],
        input_output_aliases={i: 2 * ng + i for i in range(na)},
        compiler_params=pltpu.CompilerParams(has_side_effects=DATAFLOW),
    )(*[_in_hbm(a) for a in arrays], after)
    started = []
    for g, (off, ns, nl) in enumerate(spans):
        thru = outs[2 * ng + off:2 * ng + off + ns + nl]
        started.append(dict(send=outs[2 * g], recv=outs[2 * g + 1], srcs=list(thru[:ns]), lands=list(thru[ns:]),
                            tile=outs[-1], token=outs[-1][0, 0], carry=None if carry is None else outs[2 * ng + na - 1]))
    return started


def _split_wait(name, started, copies, after):
    n, m = len(started["srcs"]), len(started["lands"])

    def body(*refs):
        src_refs, land_refs = refs[:n], refs[n:n + m]
        send_sem, recv_sem = refs[n + m], refs[n + m + 1]
        for src, dst, dev, idx in copies(src_refs, land_refs):
            cp = pltpu.make_async_remote_copy(src_ref=src, dst_ref=dst, send_sem=send_sem.at[idx], recv_sem=recv_sem.at[idx],
                                              device_id=dev, device_id_type=MESH)
            cp.wait_send()
            cp.wait_recv()

    arrays = started["srcs"] + started["lands"]
    outs = pl.pallas_call(
        body, name=name,
        in_specs=[HBM_SPEC] * (n + m) + [SEM_SPEC, SEM_SPEC, ANY],
        out_specs=[HBM_SPEC] * (n + m),
        out_shape=[pltpu.HBM(a.shape, a.dtype) for a in arrays],
        input_output_aliases={i: i for i in range(n + m)},
        compiler_params=pltpu.CompilerParams(has_side_effects=DATAFLOW),
    )(*arrays, started["send"], started["recv"], after)
    return list(outs)


def _gather_copies(srcs, lands):
    x, y, c = _position()
    chip = 2 * x + y
    targets = [(px, py, c) for px, py in _chip_peers(x, y)] + [(x, y, 1 - c)]
    return [(s, l.at[chip], dev, len(targets) * i + k) for i, (s, l) in enumerate(zip(srcs, lands)) for k, dev in enumerate(targets)]


def _sibling_copies(srcs, lands):
    x, y, c = _position()
    return [(_half_rows(srcs[0], 1 - c, srcs[0].shape[1]), lands[0], (x, y, 1 - c), 0)]


def _sibling_whole_copies(srcs, lands):
    x, y, c = _position()
    return [(srcs[0], lands[0], (x, y, 1 - c), 0)]


def _exchange_copies(srcs, lands):
    x, y, c = _position()
    return [(srcs[0].at[2 * px + py], lands[0].at[k], (px, py, c), k) for k, (px, py) in enumerate(_chip_peers(x, y))]


def _pair_sum_call(grad, recv, chip_core, name):
    _, h, B = recv.shape
    tr = _row_tile(h, B)

    def body(cc_ref, g_ref, r_ref, o_ref, ob_ref):
        s = g_ref[...] + r_ref[...]
        ob_ref[...] = s.astype(BF16)

        @pl.when(pl.program_id(1) == cc_ref[0])
        def _():
            o_ref[...] = s

    g_spec = pl.BlockSpec((None, tr, B), lambda i, q, cc_ref: (q, cc_ref[1] * (h // tr) + i, 0))
    spec = pl.BlockSpec((None, tr, B), lambda i, q, cc_ref: (q, i, 0))
    own_spec = pl.BlockSpec((tr, B), lambda i, q, cc_ref: (i, 0))
    return pl.pallas_call(
        body, name=name,
        grid_spec=pltpu.PrefetchScalarGridSpec(num_scalar_prefetch=1, grid=(h // tr, N_CHIPS), in_specs=[g_spec, spec],
                                               out_specs=[own_spec, spec]),
        out_shape=[jax.ShapeDtypeStruct((h, B), F32), jax.ShapeDtypeStruct(recv.shape, BF16)],
        compiler_params=_params(("parallel", "arbitrary")),
    )(chip_core, grad, recv)


def _chip_sum_call(partial, recv, chip_core, name):
    _, h, B = recv.shape
    tr = _row_tile(h, B)

    def body(cc_ref, p_ref, r_ref, o_ref):
        o_ref[...] = ((p_ref[...] + r_ref[0].astype(F32)) + r_ref[1].astype(F32)) + r_ref[2].astype(F32)

    return pl.pallas_call(
        body, name=name,
        grid_spec=pltpu.PrefetchScalarGridSpec(
            num_scalar_prefetch=1, grid=(h // tr,),
            in_specs=[pl.BlockSpec((tr, B), lambda i, cc_ref: (i, 0)),
                      pl.BlockSpec((3, tr, B), lambda i, cc_ref: (0, i, 0))],
            out_specs=pl.BlockSpec((tr, B), lambda i, cc_ref: (cc_ref[1] * (h // tr) + i, 0))),
        out_shape=jax.ShapeDtypeStruct((2 * h, B), F32),
        compiler_params=_params(("parallel",)),
    )(chip_core, partial, recv)


def _sibling_assemble_call(shards, name="grad_sibling_assemble"):
    n = len(shards)

    def body(*refs):
        ins, outs = refs[:n], refs[n:2 * n]
        send_sems, recv_sems = refs[2 * n:]
        x, y, c = _position()
        copies = []
        for i in range(n):
            rows = shards[i].shape[0]
            cp = pltpu.make_async_remote_copy(src_ref=_half_rows(ins[i], c, rows), dst_ref=_half_rows(outs[i], c, rows),
                                              send_sem=send_sems.at[i], recv_sem=recv_sems.at[i],
                                              device_id=(x, y, 1 - c), device_id_type=MESH)
            cp.start()
            copies.append(cp)
        for cp in copies:
            cp.wait()

    return pl.pallas_call(
        body, name=name,
        in_specs=[ANY] * n, out_specs=[ANY] * n,
        out_shape=[jax.ShapeDtypeStruct(s.shape, F32) for s in shards],
        input_output_aliases={i: i for i in range(n)},
        scratch_shapes=[pltpu.SemaphoreType.DMA((n,)), pltpu.SemaphoreType.DMA((n,))],
    )(*shards)


N_DEVICES = 8


def _allsum_copies(srcs, lands):
    x, y, c = _position()
    me = 4 * x + 2 * y + c
    out = []
    for k in range(1, N_DEVICES):
        peer = (1 - x if k & 4 else x, 1 - y if k & 2 else y, 1 - c if k & 1 else c)
        out.append((srcs[0], lands[0].at[me], peer, k - 1))
    return out


def _ordered_sum_call(mine, landed, me_chip, shapes, sharded_cols):
    rows = mine.shape[0]
    outs = [(s[0], n) if n else s for s, n in zip(shapes, sharded_cols)]

    def body(mc_ref, x_ref, l_ref, *refs):
        acc_ref = refs[-1]
        acc = jnp.where(mc_ref[0] == 0, x_ref[...], l_ref[0])
        for d in range(1, N_DEVICES):
            acc = acc + jnp.where(mc_ref[0] == d, x_ref[...], l_ref[d])
        acc_ref[...] = acc
        first = 0
        for o_ref, (r, c), n in zip(refs[:-1], shapes, sharded_cols):
            per_row = c // LANES

            def unpack(chip, o_ref=o_ref, r=r, n=n, per_row=per_row, first=first):
                for i in range(r):
                    for j in range((n or per_row * LANES) // LANES):
                        src = first + i * per_row + chip * ((n or 0) // LANES) + j
                        o_ref[i:i + 1, j * LANES:(j + 1) * LANES] = acc_ref[src:src + 1, :]

            if n:
                for q in range(N_CHIPS):
                    pl.when(mc_ref[1] == q)(functools.partial(unpack, q))
            else:
                unpack(0)
            first += r * per_row

    results = pl.pallas_call(
        body, name="small_grad_sum",
        in_specs=[pl.BlockSpec(memory_space=pltpu.SMEM), pl.BlockSpec(memory_space=pltpu.VMEM), pl.BlockSpec(memory_space=pltpu.VMEM)],
        out_specs=[pl.BlockSpec(memory_space=pltpu.VMEM)] * len(outs),
        out_shape=[jax.ShapeDtypeStruct(s, F32) for s in outs],
        scratch_shapes=[pltpu.VMEM((rows, LANES), F32)],
    )(me_chip, mine, landed)
    return results


def _pack(arrays):
    flat = jnp.concatenate([a.reshape(-1).astype(F32) for a in arrays])
    rows = -(-flat.shape[0] // LANES)
    rows = -(-rows // SUBLANES) * SUBLANES
    flat = jnp.pad(flat, (0, rows * LANES - flat.shape[0]))
    return flat.reshape(rows, LANES)


def _local_step(xs, target, P, late_weights, on_grad):
    S, D = xs.shape
    qkv_width = 3 * N_HEADS * HEAD_DIM
    glu_col0, gate_col0 = qkv_width, qkv_width + 2 * D
    shard_major = lambda g: g.reshape(N_CHIPS, g.shape[0] // N_CHIPS, g.shape[1])

    h1 = _rms_fwd_call(xs, P["norm_mix_pre"])
    buckets = _bucket_tables()
    bias = _bias_table_call(P["rel_bias"] + 0.0 * h1[0, 0].astype(F32), buckets)
    P = dict(P, **late_weights("in", bias))
    proj = _matmul(h1, P["w_in"], "nn", "proj_in")
    parts = []
    for g in range(N_GROUPS):
        parts += _attn_fwd_call(proj, bias, g)
    a, a_bf, lse = _attn_merge_call(parts)
    P = dict(P, **late_weights("mix", a_bf))
    y_a = _matmul(a_bf, P["w_attn_out"], "nn", "attn_out")
    c1 = _conv_fwd_call(proj, glu_col0, P["conv_dw_w"], P["conv_dw_b"])
    cact = _ln_silu_call(c1, P["conv_ln_g"], P["conv_ln_b"])
    y_c = _matmul(cact, P["conv_pw_w"], "nn", "conv_pw")
    mixed = _mix_call(proj, gate_col0, P["b_gate"], y_a, y_c)
    out = _matmul(mixed, P["w_out"], "nn", "mix_out")
    x1, h2 = _res1_call(xs, out, P["norm_mix_post"], P["norm_ffn_pre"])
    P = dict(P, **late_weights("up", h2))
    u = _matmul(h2, P["w_up"], "nn", "ffn_up")
    f = _ffn_fwd_call(u, P["ffn_conv_w"], P["ffn_conv_b"])
    P = dict(P, **late_weights("down", f))
    yff = _matmul(f, P["w_down"], "nn", "ffn_down")
    loss_tile, dx2, dyff, dg_ffn_post = _loss_call(yff, x1, P["norm_ffn_post"], target)

    G = {}
    G["norm_ffn_post"] = dg_ffn_post
    on_grad("w_down", shard_major(_matmul(f, dyff, "tn", "ffn_down_dw")))
    df = _matmul(dyff, P["w_down"], "nt", "ffn_down_dx")
    du, dwg, dwv, dbg, dbv = _ffn_bwd_call(u, P["ffn_conv_w"], P["ffn_conv_b"], df)
    G["ffn_conv_w"] = jnp.concatenate([dwg[:FFN_CONV_WIDTH], dwv[:FFN_CONV_WIDTH]], axis=1)
    G["ffn_conv_b"] = jnp.concatenate([dbg, dbv], axis=1)
    du = on_grad("w_up", functools.partial(_grad_half_matmul, h2, "ffn_up_dw"), carry=du)
    dh2 = _matmul(du, P["w_up"], "nt", "ffn_up_dx")
    dx1, dout, G["norm_ffn_pre"], G["norm_mix_post"] = _mid_bwd_call(x1, P["norm_ffn_pre"], dh2, dx2, out, P["norm_mix_post"])
    on_grad("w_out", shard_major(_matmul(mixed, dout, "tn", "mix_out_dw")))
    dmixed = _matmul(dout, P["w_out"], "nt", "mix_out_dx")
    dya, dyc, dproj, dba, dbc = _mix_bwd_call(dmixed, proj, gate_col0, P["b_gate"], y_a, y_c)
    G["b_gate"] = jnp.concatenate([dba, dbc], axis=1)
    on_grad("w_attn_out", _matmul(a_bf, dya, "tn", "attn_out_dw", out_shards=True))
    dyc = on_grad("conv_pw_w", shard_major(_matmul(cact, dyc, "tn", "conv_pw_dw")), carry=dyc)
    da = _matmul(dya, P["w_attn_out"], "nt", "attn_out_dx")
    dcact = _matmul(dyc, P["conv_pw_w"], "nt", "conv_pw_dx")
    dc1, G["conv_ln_g"], G["conv_ln_b"] = _ln_silu_bwd_call(c1, P["conv_ln_g"], P["conv_ln_b"], dcact)
    dproj, dw_dw, G["conv_dw_b"] = _conv_bwd_call(proj, glu_col0, P["conv_dw_w"], dc1, dproj)
    G["conv_dw_w"] = dw_dw[:CONV_WIDTH]
    delta = _attn_delta_call(a, da)
    dbs = []
    for g in range(N_GROUPS):
        dproj, db = _attn_bwd_call(proj, bias, da, lse, delta, g, dproj)
        dbs.append(db)
    G["rel_bias"] = _bias_grad_call(jnp.concatenate(dbs, axis=0), buckets)
    dproj = on_grad("w_in", functools.partial(_grad_half_matmul, h1, "proj_in_dw"), carry=dproj)
    dh1 = _matmul(dproj, P["w_in"], "nt", "proj_in_dx")
    dh1 = on_grad(None, None, carry=dh1)
    grad_x, G["norm_mix_pre"] = _in_bwd_call(xs, P["norm_mix_pre"], dh1, dx1)
    return loss_tile, grad_x, G


def kernel(x, w_in, b_gate, rel_bias, w_attn_out, conv_dw_w, conv_dw_b, conv_ln_g, conv_ln_b, conv_pw_w, w_out, norm_mix_pre, norm_mix_post, norm_ffn_pre, norm_ffn_post, w_up, ffn_conv_w, ffn_conv_b, w_down, loss_target, m_w_in, m_b_gate, m_rel_bias, m_w_attn_out, m_conv_dw_w, m_conv_dw_b, m_conv_ln_g, m_conv_ln_b, m_conv_pw_w, m_w_out, m_norm_mix_pre, m_norm_mix_post, m_norm_ffn_pre, m_norm_ffn_post, m_w_up, m_ffn_conv_w, m_ffn_conv_b, m_w_down, v_w_in, v_b_gate, v_rel_bias, v_w_attn_out, v_conv_dw_w, v_conv_dw_b, v_conv_ln_g, v_conv_ln_b, v_conv_pw_w, v_w_out, v_norm_mix_pre, v_norm_mix_post, v_norm_ffn_pre, v_norm_ffn_post, v_w_up, v_ffn_conv_w, v_ffn_conv_b, v_w_down):
    weights = dict(w_in=w_in, b_gate=b_gate, rel_bias=rel_bias, w_attn_out=w_attn_out, conv_dw_w=conv_dw_w, conv_dw_b=conv_dw_b,
                   conv_ln_g=conv_ln_g, conv_ln_b=conv_ln_b, conv_pw_w=conv_pw_w, w_out=w_out, norm_mix_pre=norm_mix_pre,
                   norm_mix_post=norm_mix_post, norm_ffn_pre=norm_ffn_pre, norm_ffn_post=norm_ffn_post, w_up=w_up,
                   ffn_conv_w=ffn_conv_w, ffn_conv_b=ffn_conv_b, w_down=w_down)
    m_in = dict(w_in=m_w_in, b_gate=m_b_gate, rel_bias=m_rel_bias, w_attn_out=m_w_attn_out, conv_dw_w=m_conv_dw_w,
                conv_dw_b=m_conv_dw_b, conv_ln_g=m_conv_ln_g, conv_ln_b=m_conv_ln_b, conv_pw_w=m_conv_pw_w, w_out=m_w_out,
                norm_mix_pre=m_norm_mix_pre, norm_mix_post=m_norm_mix_post, norm_ffn_pre=m_norm_ffn_pre,
                norm_ffn_post=m_norm_ffn_post, w_up=m_w_up, ffn_conv_w=m_ffn_conv_w, ffn_conv_b=m_ffn_conv_b, w_down=m_w_down)
    v_in = dict(w_in=v_w_in, b_gate=v_b_gate, rel_bias=v_rel_bias, w_attn_out=v_w_attn_out, conv_dw_w=v_conv_dw_w,
                conv_dw_b=v_conv_dw_b, conv_ln_g=v_conv_ln_g, conv_ln_b=v_conv_ln_b, conv_pw_w=v_conv_pw_w, w_out=v_w_out,
                norm_mix_pre=v_norm_mix_pre, norm_mix_post=v_norm_mix_post, norm_ffn_pre=v_norm_ffn_pre,
                norm_ffn_post=v_norm_ffn_post, w_up=v_w_up, ffn_conv_w=v_ffn_conv_w, ffn_conv_b=v_ffn_conv_b, w_down=v_w_down)
    names = list(weights)
    xi, yi, ci = _position()
    chip = 2 * xi + yi
    core_arr = jnp.reshape(ci, (1,)).astype(jnp.int32)

    xs = x[0]
    target = loss_target[0]
    S, D = xs.shape

    big = ["w_in", "w_attn_out", "conv_pw_w", "w_out", "w_up", "w_down"]
    row_sharded = ("conv_pw_w", "w_out", "w_down")
    natural = lambda k, g: g.reshape(-1, g.shape[2]) if k in row_sharded else g
    first_srcs = [w_in[0].astype(BF16), conv_dw_w[0], ffn_conv_w[0]]
    first_lands = [lax.empty((N_CHIPS,) + s.shape, s.dtype) for s in first_srcs]
    (first_hop,) = _split_start("gather_in_start", [(first_srcs, first_lands, 4 * len(first_srcs), _first_hop_copies)], core_arr)
    launched = first_hop["token"]
    late_sets = dict(mix=["w_attn_out", "conv_pw_w", "w_out"], up=["w_up"], down=["w_down"])
    late_groups = []
    for keys in late_sets.values():
        srcs = [(weights[k][0] + launched).astype(BF16) for k in keys]
        late_groups.append((srcs, [lax.empty((N_CHIPS,) + s.shape, BF16) for s in srcs], 4 * len(keys), _gather_copies))
    started = {}

    def late_weights(tag, after):
        if tag == "in":
            w_in_halves, dw4, fc4 = _split_wait("gather_in_wait", first_hop, _first_hop_copies, after)[len(first_srcs):]
            second_hop, *late = _split_start("gather_in_pass_start", [([], [w_in_halves], 3, _second_hop_copies)] + late_groups, dw4)
            started.update(zip(late_sets, late))
            (w_in_full,) = _split_wait("gather_in_pass_wait", second_hop, _second_hop_copies, second_hop["tile"])
            return dict(w_in=w_in_full, conv_dw_w=jnp.concatenate(list(dw4), axis=1), ffn_conv_w=jnp.concatenate(list(fc4), axis=1))
        landed = _split_wait(f"gather_{tag}_wait", started[tag], _gather_copies, after)[len(late_sets[tag]):]
        return {k: natural(k, g) for k, g in zip(late_sets[tag], landed)}

    chip_core = jnp.stack([chip, ci]).astype(jnp.int32)
    exchanging, pending, second_half = {}, {}, {}

    held = []

    def launch(tag, after, carry=None):
        keys, groups, partial = [], [], {}
        for k in list(exchanging):
            st, copies = exchanging.pop(k)
            gk, r1 = _split_wait(f"sibling_exchange_wait_{k}", st, copies, after)
            if k in second_half:
                partial[k], s16 = second_half.pop(k)(init=r1)
            else:
                partial[k], s16 = _pair_sum_call(gk, r1, chip_core, f"pair_sum_{k}")
            keys.append(k)
            groups.append(([s16], [lax.empty((3,) + s16.shape[1:], BF16)], 3, _exchange_copies))
        fresh = [(k, copies) for k, _, copies in held]
        for _, g3, copies in held:
            rows = g3.shape[1] // 2 if copies is _sibling_copies else g3.shape[1]
            groups.append(([g3], [lax.empty((N_CHIPS, rows, g3.shape[2]), F32)], 1, copies))
        held.clear()
        begun = _split_start(f"grad_exchange_start_{tag}", groups, core_arr, carry)
        for k, st in zip(keys, begun):
            pending[k] = (partial[k], st)
        for (k, copies), st in zip(fresh, begun[len(keys):]):
            exchanging[k] = (st, copies)
        return begun[0]["carry"]

    def on_grad(k, g, carry=None):
        if k is None:
            return launch("last", carry[:SUBLANES, :LANES], carry)
        if callable(g):
            theirs = g(jnp.stack([chip, 1 - ci]).astype(jnp.int32), carry)
            held.append((k, theirs, _sibling_whole_copies))
            carried = launch(k, theirs[0, :SUBLANES, :LANES], carry)
            second_half[k] = functools.partial(g, chip_core, carried)
            return carried
        held.append((k, g, _sibling_copies))
        if k in ("w_down", "w_out", "w_attn_out"):
            return carry
        return launch(k, g[0, :SUBLANES, :LANES], carry)

    def finish(keys, after, tag):
        halves = []
        for k in keys:
            s32, st = pending[k]
            recv2 = _split_wait(f"chip_exchange_wait_{k}", st, _exchange_copies, after)[1]
            halves.append(_chip_sum_call(s32, recv2, chip_core, f"chip_sum_{k}"))
        return dict(zip(keys, _sibling_assemble_call(halves, f"grad_sibling_assemble_{tag}")))

    P = dict(b_gate=b_gate, rel_bias=rel_bias, conv_dw_b=conv_dw_b, conv_ln_g=conv_ln_g, conv_ln_b=conv_ln_b,
             norm_mix_pre=norm_mix_pre + launched, norm_mix_post=norm_mix_post, norm_ffn_pre=norm_ffn_pre,
             norm_ffn_post=norm_ffn_post, ffn_conv_b=ffn_conv_b)
    loss_tile, grad_x, G = _local_step(xs, target, P, late_weights, on_grad)

    small = [k for k in names if k not in big]
    packed = _pack([loss_tile[:1]] + [G[k] for k in small])
    (allsum,) = _split_start("small_grad_allsum_start",
                             [([packed], [jnp.zeros((N_DEVICES,) + packed.shape, F32)], N_DEVICES - 1, _allsum_copies)], core_arr)

    reduced, grads, deltas, new_m, new_v = {}, {}, {}, {}, {}

    def update(keys):
        for k in keys:
            gk, d, mn, vn = _adamw_call(weights[k][0], reduced[k], m_in[k][0], v_in[k][0], f"adamw_{k}")
            grads[k], deltas[k], new_m[k], new_v[k] = gk[None], d[None], mn[None], vn[None]

    others = [k for k in big if k != "w_in"]
    reduced.update(finish(others, allsum["tile"], "others"))
    update(others)
    reduced.update(finish(["w_in"], deltas["w_up"], "w_in"))
    update(["w_in"])

    me_chip = jnp.stack([4 * xi + 2 * yi + ci, chip]).astype(jnp.int32)
    mine, landed = _split_wait("small_grad_allsum_wait", allsum, _allsum_copies, deltas["w_in"])
    piece_shapes = [(1, LANES)] + [(G[k].size // LANES, LANES) if k == "rel_bias" else G[k].shape for k in small]
    piece_cols = [0] + [weights[k].shape[2] if k in ("conv_dw_w", "ffn_conv_w") else 0 for k in small]
    loss_row, *summed = _ordered_sum_call(mine, landed, me_chip, piece_shapes, piece_cols)
    loss = loss_row[0, 0]
    for k, gsum in zip(small, summed):
        grads[k] = gsum.reshape(weights[k].shape)
    ds, mns, vns = _adamw_small_call([weights[k] for k in small], [grads[k] for k in small],
                                     [m_in[k] for k in small], [v_in[k] for k in small])
    deltas.update(zip(small, ds))
    new_m.update(zip(small, mns))
    new_v.update(zip(small, vns))

    return (loss, grad_x[None], *[grads[k] for k in names], *[deltas[k] for k in names],
            *[new_m[k] for k in names], *[new_v[k] for k in names])
```

```python
import functools
import math

import jax
import jax.numpy as jnp
import numpy as np
from jax import lax
from jax.experimental import pallas as pl
from jax.experimental.pallas import tpu as pltpu

F32 = jnp.float32
BF16 = jnp.bfloat16
MESH = pl.DeviceIdType.MESH

HEAD_DIM = 128
HEADS_PER_GROUP = 4
DILATED_PATTERNS = ((128, 1), (512, 4), (2048, 16))
N_GROUPS = 3
N_HEADS = N_GROUPS * HEADS_PER_GROUP
SPAN = 128
GROUP_WIDTH = HEADS_PER_GROUP * HEAD_DIM
CONV_WIDTH = 31
FFN_CONV_WIDTH = 3
N_BUCKETS = 32
MAX_DISTANCE = 2048
RMS_EPS = 1e-6
LN_EPS = 1e-5
NEG_INF = -1e30
ADAM_LR = 0.001
ADAM_B1 = 0.9
ADAM_B2 = 0.999
ADAM_EPS = 1e-08
ADAM_WD = 0.01
ADAM_STEP = 10

LANES = 128
SUBLANES = 8
PACKED_ROWS = 16
ROW_TILE = 512
GATE_ROWS, GATE_COLS = 512, 512
TIME_BLOCK = 128
CONV_PAD = 32
FFN_PAD = 8
VMEM_LIMIT = 56 << 20


def _params(sem=None, vmem=None):
    kw = {}
    if sem is not None:
        kw["dimension_semantics"] = sem
    if vmem is not None:
        kw["vmem_limit_bytes"] = vmem
    return pltpu.CompilerParams(**kw)


def _pick(n, cands):
    for c in cands:
        if n % c == 0:
            return c
    return n


ELEMENTWISE_TILE_BYTES = 3 << 19


def _row_tile(rows, cols):
    for align in (16, SUBLANES):
        fits = [t for t in range(align, rows + 1, align) if rows % t == 0 and t * cols * 4 <= ELEMENTWISE_TILE_BYTES]
        if fits:
            return max(fits)
    return SUBLANES


N_CHIPS = 4
M_TILES = (1024, 1408, 512, 256, 128)
N_TILES = (1024, 512, 1408, 256, 128)
K_TILES = (2176, 2048, 1408, 1024, 512, 256, 128)


def _matmul(a, b, mode, name, out_shards=False, tm=None):
    assert a.dtype == BF16 and b.dtype == BF16, (name, a.dtype, b.dtype)
    b3 = b.ndim == 3
    tn = tk = None
    halves = None
    if mode == "nn":
        M, K = a.shape
        N = b.shape[-1] * (N_CHIPS if b3 else 1)
        tn = b.shape[-1] if b3 else None
    elif mode == "nt":
        if a.ndim == 3:
            halves = a.shape[2]
        M, K = a.shape[-2], a.shape[-1] * (a.shape[0] if a.ndim == 3 else 1)
        N = b.shape[-2]
        tk = b.shape[-1] if b3 else None
    else:
        if b3:
            halves = b.shape[2]
        K, M = a.shape
        N = b.shape[-1] * (b.shape[0] if b3 else 1)
        tn = N // N_CHIPS if out_shards else None
    tm = tm or _pick(M, M_TILES)
    tn = tn or _pick(N, N_TILES)
    tk = tk or _pick(K, K_TILES)
    nk = K // tk
    dn = {"nn": (((1,), (0,)), ((), ())), "nt": (((1,), (1,)), ((), ())), "tn": (((0,), (0,)), ((), ()))}[mode]

    def body(a_ref, b_ref, o_ref):
        if nk == 1:
            o_ref[...] = lax.dot_general(a_ref[...], b_ref[...], dn, preferred_element_type=F32)
        else:
            @pl.when(pl.program_id(2) == 0)
            def _():
                o_ref[...] = jnp.zeros_like(o_ref)

            o_ref[...] += lax.dot_general(a_ref[...], b_ref[...], dn, preferred_element_type=F32)

    if mode == "tn":
        a_spec = pl.BlockSpec((tk, tm), lambda i, j, k: (k, i))
    elif halves:
        per = halves // tk
        a_spec = pl.BlockSpec((None, tm, tk), lambda i, j, k: (k // per, i, k % per))
    else:
        a_spec = pl.BlockSpec((tm, tk), lambda i, j, k: (i, k))
    if mode == "nn":
        b_spec = pl.BlockSpec((None, tk, tn), lambda i, j, k: (j, k, 0)) if b3 else pl.BlockSpec((tk, tn), lambda i, j, k: (k, j))
    elif mode == "nt":
        b_spec = pl.BlockSpec((None, tn, tk), lambda i, j, k: (k, j, 0)) if b3 else pl.BlockSpec((tn, tk), lambda i, j, k: (j, k))
    elif halves:
        per = halves // tn
        b_spec = pl.BlockSpec((None, tk, tn), lambda i, j, k: (j // per, k, j % per))
    else:
        b_spec = pl.BlockSpec((tk, tn), lambda i, j, k: (k, j))
    if out_shards:
        out_spec = pl.BlockSpec((None, tm, tn), lambda i, j, k: (j, i, 0))
        out_shape = jax.ShapeDtypeStruct((N_CHIPS, M, tn), F32)
    else:
        out_spec = pl.BlockSpec((tm, tn), lambda i, j, k: (i, j))
        out_shape = jax.ShapeDtypeStruct((M, N), F32)
    return pl.pallas_call(
        body, name=name, grid=(M // tm, N // tn, nk),
        in_specs=[a_spec, b_spec], out_specs=out_spec, out_shape=out_shape,
        compiler_params=_params(("parallel", "parallel", "arbitrary"), VMEM_LIMIT),
    )(a, b)


def _grad_half_matmul(a, name, chip_half, b, init=None):
    K, M = a.shape
    parts = b.ndim == 3
    N = b.shape[-1] * (b.shape[0] if parts else 1)
    h, tn = M // 2, N // N_CHIPS
    summed = init is not None
    dn = (((0,), (0,)), ((), ()))

    def body(ch_ref, a_ref, b_ref, *rest):
        product = lax.dot_general(a_ref[...], b_ref[...], dn, preferred_element_type=F32)
        if not summed:
            rest[0][...] = product
            return
        init_ref, own_ref, sum16_ref = rest
        total = product + init_ref[...]
        sum16_ref[...] = total.astype(BF16)

        @pl.when(pl.program_id(0) == ch_ref[0])
        def _():
            own_ref[...] = total

    if parts:
        per = b.shape[2] // tn
        b_spec = pl.BlockSpec((None, K, tn), lambda j, ch_ref: (j // per, 0, j % per))
    else:
        b_spec = pl.BlockSpec((K, tn), lambda j, ch_ref: (0, j))
    shard_spec = pl.BlockSpec((None, h, tn), lambda j, ch_ref: (j, 0, 0))
    own_spec = pl.BlockSpec((h, tn), lambda j, ch_ref: (0, 0))
    shape = (N_CHIPS, h, tn)
    return pl.pallas_call(
        body, name=name + ("_mine" if summed else "_theirs"),
        grid_spec=pltpu.PrefetchScalarGridSpec(
            num_scalar_prefetch=1, grid=(N_CHIPS,),
            in_specs=[pl.BlockSpec((K, h), lambda j, ch_ref: (0, ch_ref[1])), b_spec] + [shard_spec] * summed,
            out_specs=[own_spec, shard_spec] if summed else shard_spec),
        out_shape=[jax.ShapeDtypeStruct((h, tn), F32), jax.ShapeDtypeStruct(shape, BF16)] if summed
        else jax.ShapeDtypeStruct(shape, F32),
        compiler_params=_params(("arbitrary",), VMEM_LIMIT),
    )(chip_half, a, b, *([init] if summed else []))


def _rms(x, g):
    r = lax.rsqrt(jnp.mean(x * x, axis=-1, keepdims=True) + RMS_EPS)
    return x * r * g


def _rms_bwd(x, g, dy):
    r = lax.rsqrt(jnp.mean(x * x, axis=-1, keepdims=True) + RMS_EPS)
    n = x * r
    dn = dy * g
    dx = r * (dn - n * jnp.mean(dn * n, axis=-1, keepdims=True))
    return dx, jnp.sum(dy * n, axis=0, keepdims=True)


def _sigmoid(x):
    return 1.0 / (1.0 + jnp.exp(-x))


_GELU_C = math.sqrt(2.0 / math.pi)


def _gelu(x):
    return 0.5 * x * (1.0 + jnp.tanh(_GELU_C * (x + 0.044715 * x * x * x)))


def _gelu_and_grad(x):
    x2 = x * x
    t = jnp.tanh(_GELU_C * x * (1.0 + 0.044715 * x2))
    half = 0.5 * (1.0 + t)
    return x * half, half + (0.5 * _GELU_C) * x * (1.0 - t * t) * (1.0 + (3.0 * 0.044715) * x2)


def _row_spec(width, col_block=0):
    return pl.BlockSpec((ROW_TILE, width), lambda i: (i, col_block))


def _vec_spec(width, col_block=0):
    return pl.BlockSpec((1, width), lambda i: (0, col_block))


def _accumulate(ref, part):
    @pl.when(pl.program_id(0) == 0)
    def _():
        ref[...] = part

    @pl.when(pl.program_id(0) > 0)
    def _():
        ref[...] += part


def _rms_fwd_call(x, g):
    S, D = x.shape

    def body(x_ref, g_ref, h_ref):
        h_ref[...] = _rms(x_ref[...], g_ref[...]).astype(BF16)

    return pl.pallas_call(
        body, name="rms_mix_pre", grid=(S // ROW_TILE,),
        in_specs=[_row_spec(D), _vec_spec(D)], out_specs=_row_spec(D),
        out_shape=jax.ShapeDtypeStruct((S, D), BF16),
        compiler_params=_params(("parallel",)),
    )(x, g)


def _ln_silu_call(c1, g, b):
    S, C = c1.shape

    def body(c_ref, g_ref, b_ref, o_ref):
        xv = c_ref[...]
        mu = jnp.mean(xv, axis=-1, keepdims=True)
        xc = xv - mu
        var = jnp.mean(xc * xc, axis=-1, keepdims=True)
        z = xc * lax.rsqrt(var + LN_EPS) * g_ref[...] + b_ref[...]
        o_ref[...] = (z * _sigmoid(z)).astype(BF16)

    return pl.pallas_call(
        body, name="conv_ln_silu", grid=(S // ROW_TILE,),
        in_specs=[_row_spec(C), _vec_spec(C), _vec_spec(C)], out_specs=_row_spec(C),
        out_shape=jax.ShapeDtypeStruct((S, C), BF16),
        compiler_params=_params(("parallel",)),
    )(c1, g, b)


def _ln_silu_bwd_call(c1, g, b, dc):
    S, C = c1.shape

    def body(c_ref, g_ref, b_ref, dc_ref, dx_ref, dg_ref, db_ref):
        xv = c_ref[...]
        mu = jnp.mean(xv, axis=-1, keepdims=True)
        xc = xv - mu
        rs = lax.rsqrt(jnp.mean(xc * xc, axis=-1, keepdims=True) + LN_EPS)
        xh = xc * rs
        z = xh * g_ref[...] + b_ref[...]
        sg = _sigmoid(z)
        dz = dc_ref[...] * (sg * (1.0 + z * (1.0 - sg)))
        dxh = dz * g_ref[...]
        dx_ref[...] = rs * (dxh - jnp.mean(dxh, axis=-1, keepdims=True) - xh * jnp.mean(dxh * xh, axis=-1, keepdims=True))
        _accumulate(dg_ref, jnp.sum(dz * xh, axis=0, keepdims=True))
        _accumulate(db_ref, jnp.sum(dz, axis=0, keepdims=True))

    return pl.pallas_call(
        body, name="conv_ln_silu_bwd", grid=(S // ROW_TILE,),
        in_specs=[_row_spec(C), _vec_spec(C), _vec_spec(C), _row_spec(C)],
        out_specs=[_row_spec(C), _vec_spec(C), _vec_spec(C)],
        out_shape=[jax.ShapeDtypeStruct((S, C), F32), jax.ShapeDtypeStruct((1, C), F32), jax.ShapeDtypeStruct((1, C), F32)],
        compiler_params=_params(("arbitrary",)),
    )(c1, g, b, dc)


def _mix_call(proj, gate_col0, b_gate, y_a, y_c):
    S, D = y_a.shape
    w = GATE_COLS
    nc = D // w
    ga0, gc0 = gate_col0 // w, (gate_col0 + D) // w

    def body(ga_ref, gc_ref, ba_ref, bc_ref, ya_ref, yc_ref, o_ref):
        o_ref[...] = (_sigmoid(ga_ref[...] + ba_ref[...]) * ya_ref[...]
                      + _sigmoid(gc_ref[...] + bc_ref[...]) * yc_ref[...]).astype(BF16)

    tile = lambda off: pl.BlockSpec((GATE_ROWS, w), lambda i, j: (i, off + j))
    vec = lambda off: pl.BlockSpec((1, w), lambda i, j: (0, off + j))
    return pl.pallas_call(
        body, name="gate_mix", grid=(S // GATE_ROWS, nc),
        in_specs=[tile(ga0), tile(gc0), vec(0), vec(nc), tile(0), tile(0)],
        out_specs=tile(0), out_shape=jax.ShapeDtypeStruct((S, D), BF16),
        compiler_params=_params(("parallel", "parallel")),
    )(proj, proj, b_gate, b_gate, y_a, y_c)


def _window_stores(stage_ref, slot, dst_ref, rows, cols, sems):
    width = stage_ref.shape[-1]
    return [pltpu.make_async_copy(stage_ref.at[slot, p], dst_ref.at[rows, pl.ds(pl.multiple_of(c, LANES), width)], sems.at[slot, p])
            for p, c in enumerate(cols)]


def _staged_window_stores(stage_ref, dst_ref, sems, step, n_steps, rows, cols, fill):
    slot = step % 2
    copies = lambda s: _window_stores(stage_ref, s, dst_ref, rows, cols, sems)

    @pl.when(step >= 2)
    def _():
        for cp in copies(slot):
            cp.wait()

    fill(slot)
    for cp in copies(slot):
        cp.start()

    @pl.when(step == n_steps - 1)
    def _():
        for s in ([slot, 1 - slot] if n_steps > 1 else [slot]):
            for cp in copies(s):
                cp.wait()


def _mix_bwd_call(dmixed, proj, gate_col0, b_gate, y_a, y_c):
    S, D = y_a.shape
    w = GATE_COLS
    nc = D // w
    nr = S // GATE_ROWS
    ga0, gc0 = gate_col0 // w, (gate_col0 + D) // w

    def body(dm_ref, ga_ref, gc_ref, ba_ref, bc_ref, ya_ref, yc_ref, dya_ref, dyc_ref, dproj_ref, dba_ref, dbc_ref,
             stage_ref, sems):
        j, i = pl.program_id(0), pl.program_id(1)
        dm = dm_ref[...]
        sa = _sigmoid(ga_ref[...] + ba_ref[...])
        sc = _sigmoid(gc_ref[...] + bc_ref[...])
        dya_ref[...] = (dm * sa).astype(BF16)
        dyc_ref[...] = (dm * sc).astype(BF16)
        dga = dm * ya_ref[...] * sa * (1.0 - sa)
        dgc = dm * yc_ref[...] * sc * (1.0 - sc)

        def fill(slot):
            stage_ref[slot, 0] = dga.astype(BF16)
            stage_ref[slot, 1] = dgc.astype(BF16)

        rows = pl.ds(pl.multiple_of(i * GATE_ROWS, GATE_ROWS), GATE_ROWS)
        _staged_window_stores(stage_ref, dproj_ref, sems, j * nr + i, nc * nr, rows,
                              [gate_col0 + j * w, gate_col0 + D + j * w], fill)
        pa = jnp.sum(dga, axis=0, keepdims=True)
        pc = jnp.sum(dgc, axis=0, keepdims=True)

        @pl.when(i == 0)
        def _():
            dba_ref[...] = pa
            dbc_ref[...] = pc

        @pl.when(i > 0)
        def _():
            dba_ref[...] += pa
            dbc_ref[...] += pc

    tile = lambda off: pl.BlockSpec((GATE_ROWS, w), lambda j, i: (i, off + j))
    vec = lambda off: pl.BlockSpec((1, w), lambda j, i: (0, off + j))
    return pl.pallas_call(
        body, name="gate_mix_bwd", grid=(nc, nr),
        in_specs=[tile(0), tile(ga0), tile(gc0), vec(0), vec(nc), tile(0), tile(0)],
        out_specs=[tile(0), tile(0), ANY, vec(0), vec(0)],
        out_shape=[jax.ShapeDtypeStruct((S, D), BF16)] * 2 + [jax.ShapeDtypeStruct((S, proj.shape[1]), BF16)] + [
                   jax.ShapeDtypeStruct((1, D), F32), jax.ShapeDtypeStruct((1, D), F32)],
        scratch_shapes=[pltpu.VMEM((2, 2, GATE_ROWS, w), BF16), pltpu.SemaphoreType.DMA((2, 2))],
        compiler_params=_params(("arbitrary", "arbitrary")),
    )(dmixed, proj, proj, b_gate, b_gate, y_a, y_c)


def _res1_call(x, out, g_post, g_pre):
    S, D = x.shape

    def body(x_ref, o_ref, gp_ref, gq_ref, x1_ref, h2_ref):
        x1 = x_ref[...] + _rms(o_ref[...], gp_ref[...])
        x1_ref[...] = x1
        h2_ref[...] = _rms(x1, gq_ref[...]).astype(BF16)

    return pl.pallas_call(
        body, name="residual_mix", grid=(S // ROW_TILE,),
        in_specs=[_row_spec(D), _row_spec(D), _vec_spec(D), _vec_spec(D)],
        out_specs=[_row_spec(D), _row_spec(D)],
        out_shape=[jax.ShapeDtypeStruct((S, D), F32), jax.ShapeDtypeStruct((S, D), BF16)],
        compiler_params=_params(("parallel",)),
    )(x, out, g_post, g_pre)


def _loss_call(y, x1, g_post, target):
    S, D = y.shape

    def body(y_ref, x1_ref, g_ref, t_ref, loss_ref, dx_ref, dy_ref, dg_ref):
        yv, gv = y_ref[...], g_ref[...]
        err = x1_ref[...] + _rms(yv, gv) - t_ref[...]
        dx2 = err * (1.0 / D)
        dx_ref[...] = dx2
        dy, dg = _rms_bwd(yv, gv, dx2)
        dy_ref[...] = dy.astype(BF16)
        _accumulate(dg_ref, dg)
        part = 0.5 * jnp.sum(jnp.mean(err * err, axis=-1, keepdims=True), axis=0, keepdims=True)
        _accumulate(loss_ref, jnp.broadcast_to(part, (SUBLANES, LANES)))

    return pl.pallas_call(
        body, name="residual_ffn_loss", grid=(S // ROW_TILE,),
        in_specs=[_row_spec(D), _row_spec(D), _vec_spec(D), _row_spec(D)],
        out_specs=[pl.BlockSpec((SUBLANES, LANES), lambda i: (0, 0)), _row_spec(D), _row_spec(D), _vec_spec(D)],
        out_shape=[jax.ShapeDtypeStruct((SUBLANES, LANES), F32), jax.ShapeDtypeStruct((S, D), F32),
                   jax.ShapeDtypeStruct((S, D), BF16), jax.ShapeDtypeStruct((1, D), F32)],
        compiler_params=_params(("arbitrary",)),
    )(y, x1, g_post, target)


def _mid_bwd_call(x1, g_pre, dh2, dx2, out, g_post):
    S, D = x1.shape

    def body(x1_ref, gq_ref, dh_ref, dx2_ref, o_ref, gp_ref, dx1_ref, do_ref, dgq_ref, dgp_ref):
        d, dgq = _rms_bwd(x1_ref[...], gq_ref[...], dh_ref[...])
        dx1 = dx2_ref[...] + d
        dx1_ref[...] = dx1
        do, dgp = _rms_bwd(o_ref[...], gp_ref[...], dx1)
        do_ref[...] = do.astype(BF16)
        _accumulate(dgq_ref, dgq)
        _accumulate(dgp_ref, dgp)

    return pl.pallas_call(
        body, name="residual_mix_bwd", grid=(S // ROW_TILE,),
        in_specs=[_row_spec(D), _vec_spec(D), _row_spec(D), _row_spec(D), _row_spec(D), _vec_spec(D)],
        out_specs=[_row_spec(D), _row_spec(D), _vec_spec(D), _vec_spec(D)],
        out_shape=[jax.ShapeDtypeStruct((S, D), F32), jax.ShapeDtypeStruct((S, D), BF16)] + [jax.ShapeDtypeStruct((1, D), F32)] * 2,
        compiler_params=_params(("arbitrary",)),
    )(x1, g_pre, dh2, dx2, out, g_post)


def _in_bwd_call(x, g, dh1, dx1):
    S, D = x.shape

    def body(x_ref, g_ref, dh_ref, dx1_ref, gx_ref, dg_ref):
        d, dg = _rms_bwd(x_ref[...], g_ref[...], dh_ref[...])
        gx_ref[...] = dx1_ref[...] + d
        _accumulate(dg_ref, dg)

    return pl.pallas_call(
        body, name="rms_mix_pre_bwd", grid=(S // ROW_TILE,),
        in_specs=[_row_spec(D), _vec_spec(D), _row_spec(D), _row_spec(D)],
        out_specs=[_row_spec(D), _vec_spec(D)],
        out_shape=[jax.ShapeDtypeStruct((S, D), F32), jax.ShapeDtypeStruct((1, D), F32)],
        compiler_params=_params(("arbitrary",)),
    )(x, g, dh1, dx1)


def _bucket_table(dilation):
    qi = np.arange(SPAN)[:, None]
    ki = np.arange(2 * SPAN)[None, :]
    dist = np.maximum(qi + SPAN - ki, 0) * dilation
    max_exact = N_BUCKETS // 2
    d = np.maximum(dist, 1).astype(np.float64)
    large = max_exact + (np.log(d / max_exact) / math.log(MAX_DISTANCE / max_exact) * (N_BUCKETS - max_exact)).astype(np.int32)
    large = np.minimum(large, N_BUCKETS - 1)
    return np.where(dist < max_exact, dist, large).astype(np.int32)


def _bucket_tables():
    return jnp.asarray(np.stack([_bucket_table(r) for _, r in DILATED_PATTERNS]))


def _bias_table_call(rel_bias, buckets):
    def body(rb_ref, bk_ref, o_ref):
        for h in range(N_HEADS):
            bk = bk_ref[h // HEADS_PER_GROUP]

            def step(b, acc):
                return jnp.where(bk == b, rb_ref[b, h], acc)

            o_ref[h] = lax.fori_loop(0, N_BUCKETS, step, jnp.zeros((SPAN, 2 * SPAN), F32))

    return pl.pallas_call(
        body, name="rel_bias_table",
        in_specs=[pl.BlockSpec(memory_space=pltpu.SMEM), pl.BlockSpec(memory_space=pltpu.VMEM)],
        out_specs=pl.BlockSpec(memory_space=pltpu.VMEM),
        out_shape=jax.ShapeDtypeStruct((N_HEADS, SPAN, 2 * SPAN), F32),
    )(rel_bias, buckets)


def _bias_grad_call(dbias, buckets):
    def body(db_ref, bk_ref, o_ref, rows_ref):
        for h in range(N_HEADS):
            bk = bk_ref[h // HEADS_PER_GROUP]
            dv = db_ref[h]

            def step(b, carry):
                rows_ref[h, b] = jnp.sum(jnp.where(bk == b, dv, 0.0), axis=0, keepdims=True)
                return carry

            lax.fori_loop(0, N_BUCKETS, step, 0)
        o_ref[...] = jnp.sum(rows_ref[...], axis=-1, keepdims=True)

    out = pl.pallas_call(
        body, name="rel_bias_grad",
        in_specs=[pl.BlockSpec(memory_space=pltpu.VMEM), pl.BlockSpec(memory_space=pltpu.VMEM)],
        out_specs=pl.BlockSpec(memory_space=pltpu.VMEM),
        out_shape=jax.ShapeDtypeStruct((N_HEADS, N_BUCKETS, 1, 1), F32),
        scratch_shapes=[pltpu.VMEM((N_HEADS, N_BUCKETS, 1, 2 * SPAN), F32)],
    )(dbias, buckets)
    return out.reshape(N_HEADS, N_BUCKETS).T


def _dot_nt(a, b):
    return lax.dot_general(a, b, (((1,), (1,)), ((), ())), preferred_element_type=F32)


def _dot_nn(a, b):
    return lax.dot_general(a, b, (((1,), (0,)), ((), ())), preferred_element_type=F32)


def _dot_tn(a, b):
    return lax.dot_general(a, b, (((0,), (0,)), ((), ())), preferred_element_type=F32)


def _band_masks(n, nb):
    qi = lax.broadcasted_iota(jnp.int32, (SPAN, SPAN), 0)
    ki = lax.broadcasted_iota(jnp.int32, (SPAN, SPAN), 1)
    prev_ok = jnp.logical_and(ki >= qi, n > 0)
    cur_ok = ki <= qi
    next_ok = jnp.logical_and(ki >= qi, n < nb - 1)
    return prev_ok, cur_ok, next_ok


def _wide_band_mask(n):
    qi = lax.broadcasted_iota(jnp.int32, (SPAN, 2 * SPAN), 0)
    ki = lax.broadcasted_iota(jnp.int32, (SPAN, 2 * SPAN), 1)
    prev_ok = jnp.logical_and(jnp.logical_and(ki < SPAN, ki >= qi), n > 0)
    cur_ok = jnp.logical_and(ki >= SPAN, ki - SPAN <= qi)
    return jnp.logical_or(prev_ok, cur_ok)


def _attn_plan(S, group):
    r = DILATED_PATTERNS[group][1]
    hp, per = (HEADS_PER_GROUP, 1) if r == 1 else (2, 4)
    return r, S // (r * SPAN), hp, per


def _residue_rows(rho, r):
    return slice(None) if r == 1 else pl.ds(rho, SPAN, stride=r)


def _for_residues(r, per, fn):
    if r == per:
        for u in range(per):
            fn(u)
        return

    def step(i, carry):
        for u in range(per):
            fn(i * per + u)
        return carry

    lax.fori_loop(0, r // per, step, 0)


def _attn_fwd_call(proj, bias, group):
    S = proj.shape[0]
    r, nb, hp, per = _attn_plan(S, group)
    scale = HEAD_DIM ** -0.5
    kinds = ("q", "kp", "kc", "vp", "vc") if nb > 1 else ("q", "kc", "vc")

    per_kind = _refs_per_kind(r, hp)

    def body(*refs):
        ins = {kind: refs[i * per_kind:(i + 1) * per_kind] for i, kind in enumerate(kinds)}
        b_ref, o_ref, lse_ref = refs[len(kinds) * per_kind:]
        n = pl.program_id(1)
        prev_ok, cur_ok, _ = _band_masks(n, nb)

        band_ok = _wide_band_mask(n) if nb > 1 else cur_ok

        def residue(rho):
            rows = _residue_rows(rho, r)
            for j in range(hp):
                get = lambda kind: _head_rows(ins[kind], j, rows, r).astype(BF16)
                q = get("q")
                if nb > 1:
                    keys, vals, bias_j = jnp.concatenate([get("kp"), get("kc")], axis=0), jnp.concatenate([get("vp"), get("vc")], axis=0), b_ref[j]
                else:
                    keys, vals, bias_j = get("kc"), get("vc"), b_ref[j, :, SPAN:]
                s = jnp.where(band_ok, _dot_nt(q, keys) * scale + bias_j, NEG_INF)
                m = jnp.max(s, axis=-1, keepdims=True)
                p = jnp.exp(s - m)
                den = jnp.sum(p, axis=-1, keepdims=True)
                o_ref[j, rows, :] = _dot_nn(p.astype(BF16), vals) / den
                lse_ref[j, rows, :] = jnp.broadcast_to(m + jnp.log(den), (SPAN, HEAD_DIM))

        _for_residues(r, per, residue)

    in_specs = [_head_spec(r, nb, hp, kind, group, jj) for kind in kinds for jj in range(per_kind)]
    in_specs.append(pl.BlockSpec((hp, SPAN, 2 * SPAN), lambda j, n: (group * (HEADS_PER_GROUP // hp) + j, 0, 0)))
    out = pl.BlockSpec((hp, r * SPAN, HEAD_DIM), lambda j, n: (j, n, 0))
    return pl.pallas_call(
        body, name=f"attn_fwd_g{group}", grid=(HEADS_PER_GROUP // hp, nb),
        in_specs=in_specs, out_specs=[out] * 2,
        out_shape=[jax.ShapeDtypeStruct((HEADS_PER_GROUP, S, HEAD_DIM), F32)] * 2,
        compiler_params=_params(("parallel", "parallel"), VMEM_LIMIT),
    )(*([proj] * (len(in_specs) - 1)), bias)


_PROJ_PART = dict(q=0, qn=0, kp=1, kc=1, vp=2, vc=2)


def _refs_per_kind(r, hp):
    return 1 if r == 1 else hp


def _head_rows(refs, j, rows, r):
    return refs[0][:, j * HEAD_DIM:(j + 1) * HEAD_DIM] if r == 1 else refs[j][rows, :]


def _head_spec(r, nb, hp, kind, group, jj):
    if kind in _PROJ_PART:
        base = (_PROJ_PART[kind] * N_GROUPS + group) * HEADS_PER_GROUP
    else:
        base = 0
    if kind.endswith("p"):
        row = lambda n: jnp.maximum(n - 1, 0)
    elif kind.endswith("n"):
        row = lambda n: jnp.minimum(n + 1, nb - 1)
    else:
        row = lambda n: n
    if r == 1:
        return pl.BlockSpec((SPAN, hp * HEAD_DIM), lambda j, n: (row(n), base // hp + j))
    return pl.BlockSpec((r * SPAN, HEAD_DIM), lambda j, n: (row(n), base + j * hp + jj))


def _attn_merge_call(parts):
    S = parts[0].shape[1]

    def body(o1, s1, o2, s2, o3, s3, a_ref, ab_ref, lse_ref):
        for j in range(HEADS_PER_GROUP):
            sl = slice(j * HEAD_DIM, (j + 1) * HEAD_DIM)
            mx = jnp.maximum(jnp.maximum(s1[j], s2[j]), s3[j])
            w1 = jnp.exp(s1[j] - mx)
            w2 = jnp.exp(s2[j] - mx)
            w3 = jnp.exp(s3[j] - mx)
            den = w1 + w2 + w3
            a = (w1 * o1[j] + w2 * o2[j] + w3 * o3[j]) / den
            a_ref[:, sl] = a
            ab_ref[:, sl] = a.astype(BF16)
            lse_ref[:, sl] = mx + jnp.log(den)

    heads = pl.BlockSpec((HEADS_PER_GROUP, ROW_TILE, HEAD_DIM), lambda i: (0, i, 0))
    return pl.pallas_call(
        body, name="attn_merge", grid=(S // ROW_TILE,),
        in_specs=[heads] * 6, out_specs=[_row_spec(GROUP_WIDTH)] * 3,
        out_shape=[jax.ShapeDtypeStruct((S, GROUP_WIDTH), F32), jax.ShapeDtypeStruct((S, GROUP_WIDTH), BF16),
                   jax.ShapeDtypeStruct((S, GROUP_WIDTH), F32)],
        compiler_params=_params(("parallel",)),
    )(*parts)


def _attn_delta_call(a, da):
    S = a.shape[0]

    def body(a_ref, da_ref, d_ref):
        for j in range(HEADS_PER_GROUP):
            sl = slice(j * HEAD_DIM, (j + 1) * HEAD_DIM)
            d = jnp.sum(a_ref[:, sl] * da_ref[:, sl], axis=-1, keepdims=True)
            d_ref[:, sl] = jnp.broadcast_to(d, (ROW_TILE, HEAD_DIM))

    return pl.pallas_call(
        body, name="attn_delta", grid=(S // ROW_TILE,),
        in_specs=[_row_spec(GROUP_WIDTH)] * 2, out_specs=_row_spec(GROUP_WIDTH),
        out_shape=jax.ShapeDtypeStruct((S, GROUP_WIDTH), F32),
        compiler_params=_params(("parallel",)),
    )(a, da)


def _attn_bwd_call(proj, bias, da, lse, delta, group, dproj):
    S = proj.shape[0]
    r, nb, hp, per = _attn_plan(S, group)
    scale = HEAD_DIM ** -0.5
    kinds = ("q", "qn", "kp", "kc", "vp", "vc", "da", "dan", "lse", "lsen", "dl", "dln") if nb > 1 else ("q", "kc", "vc", "da", "lse", "dl")
    source = dict(da=da, dan=da, lse=lse, lsen=lse, dl=delta, dln=delta)

    per_kind = _refs_per_kind(r, hp)
    per_group = HEADS_PER_GROUP // hp
    block_rows = r * SPAN

    def body(*refs):
        ins = {kind: refs[i * per_kind:(i + 1) * per_kind] for i, kind in enumerate(kinds)}
        b_ref, _, dproj_ref, db_ref, stage_ref, sems = refs[len(kinds) * per_kind:][:6]
        strided_ref = None if r == 1 else refs[-1]
        jg, n = pl.program_id(0), pl.program_id(1)
        prev_ok, cur_ok, next_ok = _band_masks(n, nb)

        @pl.when(n == 0)
        def _():
            db_ref[...] = jnp.zeros_like(db_ref)

        band_ok = _wide_band_mask(n) if nb > 1 else cur_ok

        def fill(slot):
            def put(part, j, rows, value):
                if r == 1:
                    stage_ref[slot, part, :, j * HEAD_DIM:(j + 1) * HEAD_DIM] = value.astype(BF16)
                else:
                    strided_ref[part, j, rows, :] = value

            _for_residues(r, per, functools.partial(residue, put))
            if r > 1:
                for part in range(3):
                    for j in range(hp):
                        for t0 in range(0, block_rows, ROW_TILE):
                            stage_ref[slot, part, t0:t0 + ROW_TILE, j * HEAD_DIM:(j + 1) * HEAD_DIM] = (
                                strided_ref[part, j, t0:t0 + ROW_TILE, :].astype(BF16))

        def residue(put, rho):
            rows = _residue_rows(rho, r)
            for j in range(hp):
                get = lambda kind: _head_rows(ins[kind], j, rows, r)
                q = get("q").astype(BF16)
                kc = get("kc").astype(BF16)
                vc = get("vc").astype(BF16)
                dav = get("da").astype(BF16)
                lse_q, dl_q = get("lse"), get("dl")
                if nb == 1:
                    pc = jnp.exp(jnp.where(cur_ok, _dot_nt(q, kc) * scale + b_ref[j, :, SPAN:], NEG_INF) - lse_q)
                    dsc = pc * (_dot_nt(dav, vc) - dl_q)
                    dsc_b = dsc.astype(BF16)
                    dq = _dot_nn(dsc_b, kc)
                    dk = _dot_tn(dsc_b, q)
                    dv = _dot_tn(pc.astype(BF16), dav)
                    db_ref[j, :, SPAN:] += dsc
                else:
                    qn = get("qn").astype(BF16)
                    dan = get("dan").astype(BF16)
                    keys = jnp.concatenate([get("kp").astype(BF16), kc], axis=0)
                    vals = jnp.concatenate([get("vp").astype(BF16), vc], axis=0)
                    wide = lambda t: jnp.concatenate([t, t], axis=1)
                    p = jnp.exp(jnp.where(band_ok, _dot_nt(q, keys) * scale + b_ref[j], NEG_INF) - wide(lse_q))
                    ds = p * (_dot_nt(dav, vals) - wide(dl_q))
                    dq = _dot_nn(ds.astype(BF16), keys)
                    db_ref[j] += ds
                    pn = jnp.exp(jnp.where(next_ok, _dot_nt(qn, kc) * scale + b_ref[j, :, :SPAN], NEG_INF) - get("lsen"))
                    dsn = pn * (_dot_nt(dan, vc) - get("dln"))
                    both = lambda cur_part, next_part: jnp.concatenate([cur_part.astype(BF16), next_part.astype(BF16)], axis=0)
                    dk = _dot_tn(both(ds[:, SPAN:], dsn), jnp.concatenate([q, qn], axis=0))
                    dv = _dot_tn(both(p[:, SPAN:], pn), jnp.concatenate([dav, dan], axis=0))
                put(0, j, rows, dq * scale)
                put(1, j, rows, dk * scale)
                put(2, j, rows, dv)

        cols = [(part * N_GROUPS + group) * GROUP_WIDTH + jg * (hp * HEAD_DIM) for part in range(3)]
        rows = pl.ds(pl.multiple_of(n * block_rows, SPAN), block_rows)
        _staged_window_stores(stage_ref, dproj_ref, sems, jg * nb + n, per_group * nb, rows, cols, fill)

    band = (hp, SPAN, 2 * SPAN)
    in_specs = [_head_spec(r, nb, hp, kind, group, jj) for kind in kinds for jj in range(per_kind)]
    in_specs += [pl.BlockSpec(band, lambda j, n: (group * per_group + j, 0, 0)), ANY]
    operands = [source.get(kind, proj) for kind in kinds for _ in range(per_kind)] + [bias, dproj]
    scratch = [pltpu.VMEM((2, 3, block_rows, hp * HEAD_DIM), BF16), pltpu.SemaphoreType.DMA((2, 3))]
    if r > 1:
        scratch.append(pltpu.VMEM((3, hp, block_rows, HEAD_DIM), F32))
    return pl.pallas_call(
        body, name=f"attn_bwd_g{group}", grid=(per_group, nb),
        in_specs=in_specs,
        out_specs=[ANY, pl.BlockSpec(band, lambda j, n: (j, 0, 0))],
        out_shape=[jax.ShapeDtypeStruct(dproj.shape, BF16), jax.ShapeDtypeStruct((HEADS_PER_GROUP, SPAN, 2 * SPAN), F32)],
        input_output_aliases={len(operands) - 1: 0},
        scratch_shapes=scratch,
        compiler_params=_params(("arbitrary", "arbitrary"), VMEM_LIMIT),
    )(*operands)


def _tap_rows(xpad_ref, t0, k, width, pad):
    return xpad_ref[pl.ds(t0 + (pad - (width - 1 - k)), TIME_BLOCK), :]


def _conv_block(xpad_ref, t0, w_ref, width, pad):
    acc = None
    for k in range(width):
        term = w_ref[k:k + 1, :] * _tap_rows(xpad_ref, t0, k, width, pad)
        acc = term if acc is None else acc + term
    return acc


def _conv_transpose_block(dpad_ref, t0, w_ref, width):
    acc = None
    for k in range(width):
        term = w_ref[k:k + 1, :] * dpad_ref[pl.ds(t0 + (width - 1 - k), TIME_BLOCK), :]
        acc = term if acc is None else acc + term
    return acc


def _conv_weight_grad(xpad_ref, t0, dy, dw_ref, width, pad):
    for k in range(width):
        dw_ref[k:k + 1, :] += jnp.sum(dy * _tap_rows(xpad_ref, t0, k, width, pad), axis=0, keepdims=True)


def _time_loop(S, step, skip_first=0, skip_last=0):
    def it(tb, carry):
        step(pl.multiple_of(tb * TIME_BLOCK, TIME_BLOCK))
        return carry

    lax.fori_loop(skip_first, S // TIME_BLOCK - skip_last, it, 0)


def _fill_head(head_ref, x_ref, pad):
    head_ref[0:pad, :] = jnp.zeros((pad, LANES), F32)
    head_ref[pad:, :] = x_ref[0:TIME_BLOCK, :]


def _fill_tail(tail_ref, x_ref, pad):
    S = x_ref.shape[0]
    tail_ref[0:TIME_BLOCK, :] = x_ref[S - TIME_BLOCK:S, :]
    tail_ref[TIME_BLOCK:, :] = jnp.zeros((pad, LANES), F32)


def _conv_fwd_call(proj, col0, w, b):
    S = proj.shape[0]
    C = w.shape[1]
    nt = C // LANES
    v0, g0 = col0 // LANES, (col0 + C) // LANES

    def body(val_ref, gate_ref, w_ref, b_ref, o_ref, pad_ref):
        pad_ref[0:CONV_PAD, :] = jnp.zeros((CONV_PAD, LANES), F32)
        pad_ref[CONV_PAD:, :] = val_ref[...] * _sigmoid(gate_ref[...])

        def step(t0):
            o_ref[pl.ds(t0, TIME_BLOCK), :] = _conv_block(pad_ref, t0, w_ref, CONV_WIDTH, CONV_PAD) + b_ref[...]

        _time_loop(S, step)

    seq = lambda off: pl.BlockSpec((S, LANES), lambda i: (0, off + i))
    return pl.pallas_call(
        body, name="conv_module", grid=(nt,),
        in_specs=[seq(v0), seq(g0), pl.BlockSpec((CONV_WIDTH, LANES), lambda i: (0, i)), pl.BlockSpec((1, LANES), lambda i: (0, i))],
        out_specs=seq(0), out_shape=jax.ShapeDtypeStruct((S, C), F32),
        scratch_shapes=[pltpu.VMEM((S + CONV_PAD, LANES), F32)],
        compiler_params=_params(("parallel",)),
    )(proj, proj, w, b)


def _conv_bwd_call(proj, col0, w, dc1, dproj):
    S = proj.shape[0]
    C = w.shape[1]
    nt = C // LANES
    v0, g0 = col0 // LANES, (col0 + C) // LANES

    def body(val_ref, gate_ref, w_ref, dy_ref, _, dproj_ref, dw_ref, db_ref, xpad_ref, tail_ref, dwacc_ref, stage_ref, sems):
        i = pl.program_id(0)
        xpad_ref[0:CONV_PAD, :] = jnp.zeros((CONV_PAD, LANES), F32)
        xpad_ref[CONV_PAD:, :] = val_ref[...] * _sigmoid(gate_ref[...])
        _fill_tail(tail_ref, dy_ref, CONV_PAD)
        dwacc_ref[...] = jnp.zeros_like(dwacc_ref)

        def fill(slot):
            def block(t0, dy_src, dy_t0):
                rows = pl.ds(t0, TIME_BLOCK)
                _conv_weight_grad(xpad_ref, t0, dy_ref[rows, :], dwacc_ref, CONV_WIDTH, CONV_PAD)
                dc0 = _conv_transpose_block(dy_src, dy_t0, w_ref, CONV_WIDTH)
                sg = _sigmoid(gate_ref[rows, :])
                stage_ref[slot, 0, rows, :] = (dc0 * sg).astype(BF16)
                stage_ref[slot, 1, rows, :] = (dc0 * val_ref[rows, :] * sg * (1.0 - sg)).astype(BF16)

            _time_loop(S, lambda t0: block(t0, dy_ref, t0), skip_last=1)
            block(S - TIME_BLOCK, tail_ref, 0)

        _staged_window_stores(stage_ref, dproj_ref, sems, i, nt, pl.ds(0, S),
                              [col0 + i * LANES, col0 + C + i * LANES], fill)
        dw_ref[...] = dwacc_ref[...]
        db_ref[...] = jnp.sum(dy_ref[...], axis=0, keepdims=True)

    seq = lambda off: pl.BlockSpec((S, LANES), lambda i: (0, off + i))
    return pl.pallas_call(
        body, name="conv_module_bwd", grid=(nt,),
        in_specs=[seq(v0), seq(g0), pl.BlockSpec((CONV_WIDTH, LANES), lambda i: (0, i)), seq(0), ANY],
        out_specs=[ANY, pl.BlockSpec((CONV_PAD, LANES), lambda i: (0, i)), pl.BlockSpec((1, LANES), lambda i: (0, i))],
        out_shape=[jax.ShapeDtypeStruct(dproj.shape, BF16),
                   jax.ShapeDtypeStruct((CONV_PAD, C), F32), jax.ShapeDtypeStruct((1, C), F32)],
        input_output_aliases={4: 0},
        scratch_shapes=[pltpu.VMEM((S + CONV_PAD, LANES), F32), pltpu.VMEM((TIME_BLOCK + CONV_PAD, LANES), F32),
                        pltpu.VMEM((CONV_PAD, LANES), F32),
                        pltpu.VMEM((2, 2, S, LANES), BF16), pltpu.SemaphoreType.DMA((2, 2))],
        compiler_params=_params(("arbitrary",)),
    )(proj, proj, w, dc1, dproj)


def _ffn_fwd_call(u, w, b):
    S, C2 = u.shape
    C = C2 // 2
    nt = C // LANES

    def body(ug_ref, uv_ref, wg_ref, wv_ref, bg_ref, bv_ref, f_ref, hg_ref, hv_ref):
        _fill_head(hg_ref, ug_ref, FFN_PAD)
        _fill_head(hv_ref, uv_ref, FFN_PAD)

        def block(t0, xg_ref, xv_ref, x_t0, pad):
            cg = _conv_block(xg_ref, x_t0, wg_ref, FFN_CONV_WIDTH, pad) + bg_ref[...]
            cv = _conv_block(xv_ref, x_t0, wv_ref, FFN_CONV_WIDTH, pad) + bv_ref[...]
            f_ref[pl.ds(t0, TIME_BLOCK), :] = (_gelu(cg) * cv).astype(BF16)

        block(0, hg_ref, hv_ref, 0, FFN_PAD)
        _time_loop(S, lambda t0: block(t0, ug_ref, uv_ref, t0, 0), skip_first=1)

    seq = lambda off: pl.BlockSpec((S, LANES), lambda i: (0, off + i))
    wsp = lambda off: pl.BlockSpec((FFN_CONV_WIDTH, LANES), lambda i: (0, off + i))
    bsp = lambda off: pl.BlockSpec((1, LANES), lambda i: (0, off + i))
    return pl.pallas_call(
        body, name="ffn_conv_geglu", grid=(nt,),
        in_specs=[seq(0), seq(nt), wsp(0), wsp(nt), bsp(0), bsp(nt)],
        out_specs=seq(0), out_shape=jax.ShapeDtypeStruct((S, C), BF16),
        scratch_shapes=[pltpu.VMEM((FFN_PAD + TIME_BLOCK, LANES), F32)] * 2,
        compiler_params=_params(("parallel",)),
    )(u, u, w, w, b, b)


def _ffn_bwd_call(u, w, b, df):
    S, C2 = u.shape
    C = C2 // 2
    nt = C // LANES

    def body(ug_ref, uv_ref, wg_ref, wv_ref, bg_ref, bv_ref, df_ref,
             du_ref, dwg_ref, dwv_ref, dbg_ref, dbv_ref,
             hg_ref, hv_ref, dg_ref, dv_ref, dwg_acc, dwv_acc, dbg_acc, dbv_acc):
        zeros = jnp.zeros((FFN_PAD, LANES), F32)
        _fill_head(hg_ref, ug_ref, FFN_PAD)
        _fill_head(hv_ref, uv_ref, FFN_PAD)
        dg_ref[S:, :] = zeros
        dv_ref[S:, :] = zeros
        dwg_acc[...] = jnp.zeros_like(dwg_acc)
        dwv_acc[...] = jnp.zeros_like(dwv_acc)
        dbg_acc[...] = jnp.zeros_like(dbg_acc)
        dbv_acc[...] = jnp.zeros_like(dbv_acc)

        def first(t0, xg_ref, xv_ref, x_t0, pad):
            rows = pl.ds(t0, TIME_BLOCK)
            cg = _conv_block(xg_ref, x_t0, wg_ref, FFN_CONV_WIDTH, pad) + bg_ref[...]
            cv = _conv_block(xv_ref, x_t0, wv_ref, FFN_CONV_WIDTH, pad) + bv_ref[...]
            dfb = df_ref[rows, :]
            gelu, gelu_grad = _gelu_and_grad(cg)
            dcg = dfb * cv * gelu_grad
            dcv = dfb * gelu
            dg_ref[rows, :] = dcg
            dv_ref[rows, :] = dcv
            _conv_weight_grad(xg_ref, x_t0, dcg, dwg_acc, FFN_CONV_WIDTH, pad)
            _conv_weight_grad(xv_ref, x_t0, dcv, dwv_acc, FFN_CONV_WIDTH, pad)
            dbg_acc[...] += jnp.sum(dcg, axis=0, keepdims=True)
            dbv_acc[...] += jnp.sum(dcv, axis=0, keepdims=True)

        def second(t0):
            rows = pl.ds(t0, TIME_BLOCK)
            du_ref[0, rows, :] = _conv_transpose_block(dg_ref, t0, wg_ref, FFN_CONV_WIDTH).astype(BF16)
            du_ref[1, rows, :] = _conv_transpose_block(dv_ref, t0, wv_ref, FFN_CONV_WIDTH).astype(BF16)

        first(0, hg_ref, hv_ref, 0, FFN_PAD)
        _time_loop(S, lambda t0: first(t0, ug_ref, uv_ref, t0, 0), skip_first=1)
        _time_loop(S, second)
        dwg_ref[...] = dwg_acc[...]
        dwv_ref[...] = dwv_acc[...]
        dbg_ref[...] = dbg_acc[...]
        dbv_ref[...] = dbv_acc[...]

    seq = lambda off: pl.BlockSpec((S, LANES), lambda i: (0, off + i))
    wsp = lambda off: pl.BlockSpec((FFN_CONV_WIDTH, LANES), lambda i: (0, off + i))
    bsp = lambda off: pl.BlockSpec((1, LANES), lambda i: (0, off + i))
    return pl.pallas_call(
        body, name="ffn_conv_geglu_bwd", grid=(nt,),
        in_specs=[seq(0), seq(nt), wsp(0), wsp(nt), bsp(0), bsp(nt), seq(0)],
        out_specs=[pl.BlockSpec((2, S, LANES), lambda i: (0, 0, i)),
                   pl.BlockSpec((SUBLANES, LANES), lambda i: (0, i)), pl.BlockSpec((SUBLANES, LANES), lambda i: (0, i)),
                   bsp(0), bsp(0)],
        out_shape=[jax.ShapeDtypeStruct((2, S, C), BF16)] + [jax.ShapeDtypeStruct((SUBLANES, C), F32)] * 2
        + [jax.ShapeDtypeStruct((1, C), F32)] * 2,
        scratch_shapes=[pltpu.VMEM((FFN_PAD + TIME_BLOCK, LANES), F32)] * 2 + [pltpu.VMEM((S + FFN_PAD, LANES), F32)] * 2
        + [pltpu.VMEM((SUBLANES, LANES), F32)] * 2
        + [pltpu.VMEM((1, LANES), F32)] * 2,
        compiler_params=_params(("parallel",)),
    )(u, u, w, w, b, b, df)


def _adamw(w_ref, g_ref, m_ref, v_ref, d_ref, mo_ref, vo_ref):
    gv = g_ref[...]
    mn = ADAM_B1 * m_ref[...] + (1.0 - ADAM_B1) * gv
    vn = ADAM_B2 * v_ref[...] + (1.0 - ADAM_B2) * (gv * gv)
    mo_ref[...] = mn
    vo_ref[...] = vn
    m_hat = mn * (1.0 / (1.0 - ADAM_B1 ** ADAM_STEP))
    v_hat = vn * (1.0 / (1.0 - ADAM_B2 ** ADAM_STEP))
    d_ref[...] = -ADAM_LR * (m_hat / (jnp.sqrt(v_hat) + ADAM_EPS) + ADAM_WD * w_ref[...])


def _adamw_call(w, g, m, v, name):
    R, C = w.shape
    tr = _row_tile(R, C)

    def body(w_ref, g_ref, m_ref, v_ref, go_ref, d_ref, mo_ref, vo_ref):
        go_ref[...] = g_ref[...]
        _adamw(w_ref, g_ref, m_ref, v_ref, d_ref, mo_ref, vo_ref)

    spec = pl.BlockSpec((tr, C), lambda i: (i, 0))
    return pl.pallas_call(
        body, name=name, grid=(R // tr,),
        in_specs=[spec] * 4, out_specs=[spec] * 4,
        out_shape=[jax.ShapeDtypeStruct((R, C), F32)] * 4,
        compiler_params=_params(("parallel",)),
    )(w, g, m, v)


def _adamw_small_call(ws, gs, ms, vs):
    n = len(ws)

    def body(*refs):
        w_refs, g_refs, m_refs, v_refs, d_refs, mo_refs, vo_refs = (refs[i * n:(i + 1) * n] for i in range(7))
        for i in range(n):
            _adamw(w_refs[i], g_refs[i], m_refs[i], v_refs[i], d_refs[i], mo_refs[i], vo_refs[i])

    whole = pl.BlockSpec(memory_space=pltpu.VMEM)
    outs = pl.pallas_call(
        body, name="adamw_small",
        in_specs=[whole] * (4 * n), out_specs=[whole] * (3 * n),
        out_shape=[jax.ShapeDtypeStruct(w.shape, F32) for w in ws] * 3,
    )(*ws, *gs, *ms, *vs)
    return outs[:n], outs[n:2 * n], outs[2 * n:]


def _position():
    return lax.axis_index("x"), lax.axis_index("y"), lax.axis_index("c")


def _chip_peers(x, y):
    return [(x, 1 - y), (1 - x, y), (1 - x, 1 - y)]


def _half_rows(ref, core, rows):
    h = rows // 2
    start = pl.multiple_of(core * h, PACKED_ROWS)
    return ref.at[pl.ds(start, h), :] if len(ref.shape) == 2 else ref.at[:, pl.ds(start, h), :]


def _shard_half(ref, shard, core, rows):
    h = rows // 2
    return ref.at[shard, pl.ds(pl.multiple_of(core * h, PACKED_ROWS), h), :]


ANY = pl.BlockSpec(memory_space=pl.ANY)


def _first_hop_copies(srcs, lands):
    x, y, c = _position()
    chip = 2 * x + y
    targets = [(px, py, c) for px, py in _chip_peers(x, y)] + [(x, y, 1 - c)]
    rows = srcs[0].shape[0]
    out = []
    for i, (s, l) in enumerate(zip(srcs, lands)):
        for k, dev in enumerate(targets):
            if i == 0 and k < 3:
                out.append((_half_rows(s, c, rows), _shard_half(l, chip, c, rows), dev, k))
            else:
                out.append((s, l.at[chip], dev, len(targets) * i + k))
    return out


def _second_hop_copies(srcs, lands):
    x, y, c = _position()
    rows = lands[0].shape[1]
    out = []
    for k, (px, py) in enumerate(_chip_peers(x, y)):
        half = _shard_half(lands[0], 2 * px + py, c, rows)
        out.append((half, half, (x, y, 1 - c), k))
    return out


HBM_SPEC = pl.BlockSpec(memory_space=pltpu.HBM)
SEM_SPEC = pl.BlockSpec(memory_space=pltpu.SEMAPHORE)
DATAFLOW = pltpu.SideEffectType.DATAFLOW_SIDE_EFFECTING


def _in_hbm(a):
    return pltpu.with_memory_space_constraint(a, pltpu.HBM)


def _split_start(name, groups, after, carry=None):
    spans, arrays = [], []
    for srcs, lands, _, _ in groups:
        spans.append((len(arrays), len(srcs), len(lands)))
        arrays += list(srcs) + list(lands)
    if carry is not None:
        arrays.append(carry)
    na, ng = len(arrays), len(groups)

    def body(*refs):
        sems, token = refs[na + 1:na + 1 + 2 * ng], refs[-1]
        for g, (_, _, _, copies) in enumerate(groups):
            off, ns, nl = spans[g]
            for src, dst, dev, idx in copies(refs[off:off + ns], refs[off + ns:off + ns + nl]):
                pltpu.make_async_remote_copy(src_ref=src, dst_ref=dst, send_sem=sems[2 * g].at[idx], recv_sem=sems[2 * g + 1].at[idx],
                                             device_id=dev, device_id_type=MESH).start()
        token[...] = jnp.zeros_like(token)

    outs = pl.pallas_call(
        body, name=name,
        in_specs=[HBM_SPEC] * na + [ANY],
        out_specs=[SEM_SPEC] * (2 * ng) + [HBM_SPEC] * na + [pl.BlockSpec(memory_space=pltpu.VMEM)],
        out_shape=[pltpu.SemaphoreType.DMA((n_sems,)) for _, _, n_sems, _ in groups for _ in range(2)]
        + [pltpu.HBM(a.shape, a.dtype) for a in arrays] + [jax.ShapeDtypeStruct((SUBLANES, LANES), F32)],
        input_output_aliases={i: 2 * ng + i for i in range(na)},
        compiler_params=pltpu.CompilerParams(has_side_effects=DATAFLOW),
    )(*[_in_hbm(a) for a in arrays], after)
    started = []
    for g, (off, ns, nl) in enumerate(spans):
        thru = outs[2 * ng + off:2 * ng + off + ns + nl]
        started.append(dict(send=outs[2 * g], recv=outs[2 * g + 1], srcs=list(thru[:ns]), lands=list(thru[ns:]),
                            tile=outs[-1], token=outs[-1][0, 0], carry=None if carry is None else outs[2 * ng + na - 1]))
    return started


def _split_wait(name, started, copies, after):
    n, m = len(started["srcs"]), len(started["lands"])
    after = list(after) if isinstance(after, (list, tuple)) else [after]

    def body(*refs):
        src_refs, land_refs = refs[:n], refs[n:n + m]
        send_sem, recv_sem = refs[n + m], refs[n + m + 1]
        for src, dst, dev, idx in copies(src_refs, land_refs):
            cp = pltpu.make_async_remote_copy(src_ref=src, dst_ref=dst, send_sem=send_sem.at[idx], recv_sem=recv_sem.at[idx],
                                              device_id=dev, device_id_type=MESH)
            cp.wait_send()
            cp.wait_recv()

    arrays = started["srcs"] + started["lands"]
    outs = pl.pallas_call(
        body, name=name,
        in_specs=[HBM_SPEC] * (n + m) + [SEM_SPEC, SEM_SPEC] + [ANY] * len(after),
        out_specs=[HBM_SPEC] * (n + m),
        out_shape=[pltpu.HBM(a.shape, a.dtype) for a in arrays],
        input_output_aliases={i: i for i in range(n + m)},
        compiler_params=pltpu.CompilerParams(has_side_effects=DATAFLOW),
    )(*arrays, started["send"], started["recv"], *after)
    return list(outs)


def _gather_copies(srcs, lands):
    x, y, c = _position()
    chip = 2 * x + y
    targets = [(px, py, c) for px, py in _chip_peers(x, y)] + [(x, y, 1 - c)]
    return [(s, l.at[chip], dev, len(targets) * i + k) for i, (s, l) in enumerate(zip(srcs, lands)) for k, dev in enumerate(targets)]


def _sibling_copies(srcs, lands):
    x, y, c = _position()
    return [(_half_rows(srcs[0], 1 - c, srcs[0].shape[1]), lands[0], (x, y, 1 - c), 0)]


def _sibling_whole_copies(srcs, lands):
    x, y, c = _position()
    return [(srcs[0], lands[0], (x, y, 1 - c), 0)]


def _exchange_copies(srcs, lands):
    x, y, c = _position()
    return [(srcs[0].at[2 * px + py], lands[0].at[k], (px, py, c), k) for k, (px, py) in enumerate(_chip_peers(x, y))]


def _pair_sum_call(grad, recv, chip_core, name):
    _, h, B = recv.shape
    tr = _row_tile(h, B)

    def body(cc_ref, g_ref, r_ref, o_ref, ob_ref):
        s = g_ref[...] + r_ref[...]
        ob_ref[...] = s.astype(BF16)

        @pl.when(pl.program_id(1) == cc_ref[0])
        def _():
            o_ref[...] = s

    g_spec = pl.BlockSpec((None, tr, B), lambda i, q, cc_ref: (q, cc_ref[1] * (h // tr) + i, 0))
    spec = pl.BlockSpec((None, tr, B), lambda i, q, cc_ref: (q, i, 0))
    own_spec = pl.BlockSpec((tr, B), lambda i, q, cc_ref: (i, 0))
    return pl.pallas_call(
        body, name=name,
        grid_spec=pltpu.PrefetchScalarGridSpec(num_scalar_prefetch=1, grid=(h // tr, N_CHIPS), in_specs=[g_spec, spec],
                                               out_specs=[own_spec, spec]),
        out_shape=[jax.ShapeDtypeStruct((h, B), F32), jax.ShapeDtypeStruct(recv.shape, BF16)],
        compiler_params=_params(("parallel", "arbitrary")),
    )(chip_core, grad, recv)


def _chip_sum_call(partial, recv, chip_core, name):
    _, h, B = recv.shape
    tr = _row_tile(h, B)

    def body(cc_ref, p_ref, r_ref, o_ref):
        o_ref[...] = ((p_ref[...] + r_ref[0].astype(F32)) + r_ref[1].astype(F32)) + r_ref[2].astype(F32)

    return pl.pallas_call(
        body, name=name,
        grid_spec=pltpu.PrefetchScalarGridSpec(
            num_scalar_prefetch=1, grid=(h // tr,),
            in_specs=[pl.BlockSpec((tr, B), lambda i, cc_ref: (i, 0)),
                      pl.BlockSpec((3, tr, B), lambda i, cc_ref: (0, i, 0))],
            out_specs=pl.BlockSpec((tr, B), lambda i, cc_ref: (cc_ref[1] * (h // tr) + i, 0))),
        out_shape=jax.ShapeDtypeStruct((2 * h, B), F32),
        compiler_params=_params(("parallel",)),
    )(chip_core, partial, recv)


def _sibling_assemble_call(shards, name="grad_sibling_assemble"):
    n = len(shards)

    def body(*refs):
        ins, outs = refs[:n], refs[n:2 * n]
        send_sems, recv_sems = refs[2 * n:]
        x, y, c = _position()
        copies = []
        for i in range(n):
            rows = shards[i].shape[0]
            cp = pltpu.make_async_remote_copy(src_ref=_half_rows(ins[i], c, rows), dst_ref=_half_rows(outs[i], c, rows),
                                              send_sem=send_sems.at[i], recv_sem=recv_sems.at[i],
                                              device_id=(x, y, 1 - c), device_id_type=MESH)
            cp.start()
            copies.append(cp)
        for cp in copies:
            cp.wait()

    return pl.pallas_call(
        body, name=name,
        in_specs=[ANY] * n, out_specs=[ANY] * n,
        out_shape=[jax.ShapeDtypeStruct(s.shape, F32) for s in shards],
        input_output_aliases={i: i for i in range(n)},
        scratch_shapes=[pltpu.SemaphoreType.DMA((n,)), pltpu.SemaphoreType.DMA((n,))],
    )(*shards)


N_DEVICES = 8


def _allsum_copies(srcs, lands):
    x, y, c = _position()
    me = 4 * x + 2 * y + c
    out = []
    for k in range(1, N_DEVICES):
        peer = (1 - x if k & 4 else x, 1 - y if k & 2 else y, 1 - c if k & 1 else c)
        out.append((srcs[0], lands[0].at[me], peer, k - 1))
    return out


def _ordered_sum_call(mine, landed, me_chip, shapes, sharded_cols):
    rows = mine.shape[0]
    outs = [(s[0], n) if n else s for s, n in zip(shapes, sharded_cols)]

    def body(mc_ref, x_ref, l_ref, *refs):
        acc_ref = refs[-1]
        acc = jnp.where(mc_ref[0] == 0, x_ref[...], l_ref[0])
        for d in range(1, N_DEVICES):
            acc = acc + jnp.where(mc_ref[0] == d, x_ref[...], l_ref[d])
        acc_ref[...] = acc
        first = 0
        for o_ref, (r, c), n in zip(refs[:-1], shapes, sharded_cols):
            per_row = c // LANES

            def unpack(chip, o_ref=o_ref, r=r, n=n, per_row=per_row, first=first):
                for i in range(r):
                    for j in range((n or per_row * LANES) // LANES):
                        src = first + i * per_row + chip * ((n or 0) // LANES) + j
                        o_ref[i:i + 1, j * LANES:(j + 1) * LANES] = acc_ref[src:src + 1, :]

            if n:
                for q in range(N_CHIPS):
                    pl.when(mc_ref[1] == q)(functools.partial(unpack, q))
            else:
                unpack(0)
            first += r * per_row

    results = pl.pallas_call(
        body, name="small_grad_sum",
        in_specs=[pl.BlockSpec(memory_space=pltpu.SMEM), pl.BlockSpec(memory_space=pltpu.VMEM), pl.BlockSpec(memory_space=pltpu.VMEM)],
        out_specs=[pl.BlockSpec(memory_space=pltpu.VMEM)] * len(outs),
        out_shape=[jax.ShapeDtypeStruct(s, F32) for s in outs],
        scratch_shapes=[pltpu.VMEM((rows, LANES), F32)],
    )(me_chip, mine, landed)
    return results


def _pack(arrays):
    flat = jnp.concatenate([a.reshape(-1).astype(F32) for a in arrays])
    rows = -(-flat.shape[0] // LANES)
    rows = -(-rows // SUBLANES) * SUBLANES
    flat = jnp.pad(flat, (0, rows * LANES - flat.shape[0]))
    return flat.reshape(rows, LANES)


def _local_step(xs, target, P, late_weights, on_grad):
    S, D = xs.shape
    qkv_width = 3 * N_HEADS * HEAD_DIM
    glu_col0, gate_col0 = qkv_width, qkv_width + 2 * D
    shard_major = lambda g: g.reshape(N_CHIPS, g.shape[0] // N_CHIPS, g.shape[1])

    h1 = _rms_fwd_call(xs, P["norm_mix_pre"])
    buckets = _bucket_tables()
    bias = _bias_table_call(P["rel_bias"] + 0.0 * h1[0, 0].astype(F32), buckets)
    P = dict(P, **late_weights("in", bias))
    proj = _matmul(h1, P["w_in"], "nn", "proj_in")
    parts = []
    for g in range(N_GROUPS):
        parts += _attn_fwd_call(proj, bias, g)
    a, a_bf, lse = _attn_merge_call(parts)
    P = dict(P, **late_weights("mix", a_bf))
    y_a = _matmul(a_bf, P["w_attn_out"], "nn", "attn_out")
    c1 = _conv_fwd_call(proj, glu_col0, P["conv_dw_w"], P["conv_dw_b"])
    cact = _ln_silu_call(c1, P["conv_ln_g"], P["conv_ln_b"])
    y_c = _matmul(cact, P["conv_pw_w"], "nn", "conv_pw")
    mixed = _mix_call(proj, gate_col0, P["b_gate"], y_a, y_c)
    out = _matmul(mixed, P["w_out"], "nn", "mix_out")
    x1, h2 = _res1_call(xs, out, P["norm_mix_post"], P["norm_ffn_pre"])
    P = dict(P, **late_weights("up", h2))
    u = _matmul(h2, P["w_up"], "nn", "ffn_up")
    f = _ffn_fwd_call(u, P["ffn_conv_w"], P["ffn_conv_b"])
    P = dict(P, **late_weights("down", f))
    yff = _matmul(f, P["w_down"], "nn", "ffn_down")
    loss_tile, dx2, dyff, dg_ffn_post = _loss_call(yff, x1, P["norm_ffn_post"], target)

    G = {}
    G["norm_ffn_post"] = dg_ffn_post
    on_grad("w_down", shard_major(_matmul(f, dyff, "tn", "ffn_down_dw")))
    df = _matmul(dyff, P["w_down"], "nt", "ffn_down_dx")
    du, dwg, dwv, dbg, dbv = _ffn_bwd_call(u, P["ffn_conv_w"], P["ffn_conv_b"], df)
    G["ffn_conv_w"] = jnp.concatenate([dwg[:FFN_CONV_WIDTH], dwv[:FFN_CONV_WIDTH]], axis=1)
    G["ffn_conv_b"] = jnp.concatenate([dbg, dbv], axis=1)
    du = on_grad("w_up", functools.partial(_grad_half_matmul, h2, "ffn_up_dw"), carry=du)
    dh2 = _matmul(du, P["w_up"], "nt", "ffn_up_dx")
    dx1, dout, G["norm_ffn_pre"], G["norm_mix_post"] = _mid_bwd_call(x1, P["norm_ffn_pre"], dh2, dx2, out, P["norm_mix_post"])
    on_grad("w_out", shard_major(_matmul(mixed, dout, "tn", "mix_out_dw")))
    dmixed = _matmul(dout, P["w_out"], "nt", "mix_out_dx")
    dya, dyc, dproj, dba, dbc = _mix_bwd_call(dmixed, proj, gate_col0, P["b_gate"], y_a, y_c)
    G["b_gate"] = jnp.concatenate([dba, dbc], axis=1)
    on_grad("w_attn_out", _matmul(a_bf, dya, "tn", "attn_out_dw", out_shards=True))
    dyc = on_grad("conv_pw_w", shard_major(_matmul(cact, dyc, "tn", "conv_pw_dw")), carry=dyc)
    da = _matmul(dya, P["w_attn_out"], "nt", "attn_out_dx")
    dcact = _matmul(dyc, P["conv_pw_w"], "nt", "conv_pw_dx")
    dc1, G["conv_ln_g"], G["conv_ln_b"] = _ln_silu_bwd_call(c1, P["conv_ln_g"], P["conv_ln_b"], dcact)
    dproj, dw_dw, G["conv_dw_b"] = _conv_bwd_call(proj, glu_col0, P["conv_dw_w"], dc1, dproj)
    G["conv_dw_w"] = dw_dw[:CONV_WIDTH]
    delta = _attn_delta_call(a, da)
    dbs = []
    for g in range(N_GROUPS):
        dproj, db = _attn_bwd_call(proj, bias, da, lse, delta, g, dproj)
        dbs.append(db)
    G["rel_bias"] = _bias_grad_call(jnp.concatenate(dbs, axis=0), buckets)
    dproj = on_grad("w_in", functools.partial(_grad_half_matmul, h1, "proj_in_dw"), carry=dproj)
    dh1 = _matmul(dproj, P["w_in"], "nt", "proj_in_dx")
    dh1 = on_grad(None, None, carry=dh1)
    grad_x, G["norm_mix_pre"] = _in_bwd_call(xs, P["norm_mix_pre"], dh1, dx1)
    return loss_tile, grad_x, G


def kernel(x, w_in, b_gate, rel_bias, w_attn_out, conv_dw_w, conv_dw_b, conv_ln_g, conv_ln_b, conv_pw_w, w_out, norm_mix_pre, norm_mix_post, norm_ffn_pre, norm_ffn_post, w_up, ffn_conv_w, ffn_conv_b, w_down, loss_target, m_w_in, m_b_gate, m_rel_bias, m_w_attn_out, m_conv_dw_w, m_conv_dw_b, m_conv_ln_g, m_conv_ln_b, m_conv_pw_w, m_w_out, m_norm_mix_pre, m_norm_mix_post, m_norm_ffn_pre, m_norm_ffn_post, m_w_up, m_ffn_conv_w, m_ffn_conv_b, m_w_down, v_w_in, v_b_gate, v_rel_bias, v_w_attn_out, v_conv_dw_w, v_conv_dw_b, v_conv_ln_g, v_conv_ln_b, v_conv_pw_w, v_w_out, v_norm_mix_pre, v_norm_mix_post, v_norm_ffn_pre, v_norm_ffn_post, v_w_up, v_ffn_conv_w, v_ffn_conv_b, v_w_down):
    weights = dict(w_in=w_in, b_gate=b_gate, rel_bias=rel_bias, w_attn_out=w_attn_out, conv_dw_w=conv_dw_w, conv_dw_b=conv_dw_b,
                   conv_ln_g=conv_ln_g, conv_ln_b=conv_ln_b, conv_pw_w=conv_pw_w, w_out=w_out, norm_mix_pre=norm_mix_pre,
                   norm_mix_post=norm_mix_post, norm_ffn_pre=norm_ffn_pre, norm_ffn_post=norm_ffn_post, w_up=w_up,
                   ffn_conv_w=ffn_conv_w, ffn_conv_b=ffn_conv_b, w_down=w_down)
    m_in = dict(w_in=m_w_in, b_gate=m_b_gate, rel_bias=m_rel_bias, w_attn_out=m_w_attn_out, conv_dw_w=m_conv_dw_w,
                conv_dw_b=m_conv_dw_b, conv_ln_g=m_conv_ln_g, conv_ln_b=m_conv_ln_b, conv_pw_w=m_conv_pw_w, w_out=m_w_out,
                norm_mix_pre=m_norm_mix_pre, norm_mix_post=m_norm_mix_post, norm_ffn_pre=m_norm_ffn_pre,
                norm_ffn_post=m_norm_ffn_post, w_up=m_w_up, ffn_conv_w=m_ffn_conv_w, ffn_conv_b=m_ffn_conv_b, w_down=m_w_down)
    v_in = dict(w_in=v_w_in, b_gate=v_b_gate, rel_bias=v_rel_bias, w_attn_out=v_w_attn_out, conv_dw_w=v_conv_dw_w,
                conv_dw_b=v_conv_dw_b, conv_ln_g=v_conv_ln_g, conv_ln_b=v_conv_ln_b, conv_pw_w=v_conv_pw_w, w_out=v_w_out,
                norm_mix_pre=v_norm_mix_pre, norm_mix_post=v_norm_mix_post, norm_ffn_pre=v_norm_ffn_pre,
                norm_ffn_post=v_norm_ffn_post, w_up=v_w_up, ffn_conv_w=v_ffn_conv_w, ffn_conv_b=v_ffn_conv_b, w_down=v_w_down)
    names = list(weights)
    xi, yi, ci = _position()
    chip = 2 * xi + yi
    core_arr = jnp.reshape(ci, (1,)).astype(jnp.int32)

    xs = x[0]
    target = loss_target[0]
    S, D = xs.shape

    big = ["w_in", "w_attn_out", "conv_pw_w", "w_out", "w_up", "w_down"]
    row_sharded = ("conv_pw_w", "w_out", "w_down")
    natural = lambda k, g: g.reshape(-1, g.shape[2]) if k in row_sharded else g
    first_srcs = [w_in[0].astype(BF16), conv_dw_w[0], ffn_conv_w[0]]
    first_lands = [lax.empty((N_CHIPS,) + s.shape, s.dtype) for s in first_srcs]
    (first_hop,) = _split_start("gather_in_start", [(first_srcs, first_lands, 4 * len(first_srcs), _first_hop_copies)], core_arr)
    launched = first_hop["token"]
    late_sets = dict(mix=["w_attn_out", "conv_pw_w", "w_out"], up=["w_up"], down=["w_down"])
    late_groups = []
    for keys in late_sets.values():
        srcs = [(weights[k][0] + launched).astype(BF16) for k in keys]
        late_groups.append((srcs, [lax.empty((N_CHIPS,) + s.shape, BF16) for s in srcs], 4 * len(keys), _gather_copies))
    started = {}

    def late_weights(tag, after):
        if tag == "in":
            casts = [s for srcs, _, _, _ in late_groups for s in srcs]
            w_in_halves, dw4, fc4 = _split_wait("gather_in_wait", first_hop, _first_hop_copies, [after] + casts)[len(first_srcs):]
            second_hop, *late = _split_start("gather_in_pass_start", [([], [w_in_halves], 3, _second_hop_copies)] + late_groups, dw4)
            started.update(zip(late_sets, late))
            (w_in_full,) = _split_wait("gather_in_pass_wait", second_hop, _second_hop_copies, second_hop["tile"])
            return dict(w_in=w_in_full, conv_dw_w=jnp.concatenate(list(dw4), axis=1), ffn_conv_w=jnp.concatenate(list(fc4), axis=1))
        landed = _split_wait(f"gather_{tag}_wait", started[tag], _gather_copies, after)[len(late_sets[tag]):]
        return {k: natural(k, g) for k, g in zip(late_sets[tag], landed)}

    chip_core = jnp.stack([chip, ci]).astype(jnp.int32)
    exchanging, pending, second_half = {}, {}, {}

    held = []

    def launch(tag, after, carry=None):
        keys, groups, partial = [], [], {}
        for k in list(exchanging):
            st, copies = exchanging.pop(k)
            gk, r1 = _split_wait(f"sibling_exchange_wait_{k}", st, copies, after)
            if k in second_half:
                partial[k], s16 = second_half.pop(k)(init=r1)
            else:
                partial[k], s16 = _pair_sum_call(gk, r1, chip_core, f"pair_sum_{k}")
            keys.append(k)
            groups.append(([s16], [lax.empty((3,) + s16.shape[1:], BF16)], 3, _exchange_copies))
        fresh = [(k, copies) for k, _, copies in held]
        for _, g3, copies in held:
            rows = g3.shape[1] // 2 if copies is _sibling_copies else g3.shape[1]
            groups.append(([g3], [lax.empty((N_CHIPS, rows, g3.shape[2]), F32)], 1, copies))
        held.clear()
        begun = _split_start(f"grad_exchange_start_{tag}", groups, core_arr, carry)
        for k, st in zip(keys, begun):
            pending[k] = (partial[k], st)
        for (k, copies), st in zip(fresh, begun[len(keys):]):
            exchanging[k] = (st, copies)
        return begun[0]["carry"]

    def on_grad(k, g, carry=None):
        if k is None:
            return launch("last", carry[:SUBLANES, :LANES], carry)
        if callable(g):
            theirs = g(jnp.stack([chip, 1 - ci]).astype(jnp.int32), carry)
            held.append((k, theirs, _sibling_whole_copies))
            carried = launch(k, theirs[0, :SUBLANES, :LANES], carry)
            second_half[k] = functools.partial(g, chip_core, carried)
            return carried
        held.append((k, g, _sibling_copies))
        if k in ("w_down", "w_out", "w_attn_out"):
            return carry
        return launch(k, g[0, :SUBLANES, :LANES], carry)

    def finish(keys, after, tag):
        halves = []
        for k in keys:
            s32, st = pending[k]
            recv2 = _split_wait(f"chip_exchange_wait_{k}", st, _exchange_copies, after)[1]
            halves.append(_chip_sum_call(s32, recv2, chip_core, f"chip_sum_{k}"))
        return dict(zip(keys, _sibling_assemble_call(halves, f"grad_sibling_assemble_{tag}")))

    P = dict(b_gate=b_gate, rel_bias=rel_bias, conv_dw_b=conv_dw_b, conv_ln_g=conv_ln_g, conv_ln_b=conv_ln_b,
             norm_mix_pre=norm_mix_pre + launched, norm_mix_post=norm_mix_post, norm_ffn_pre=norm_ffn_pre,
             norm_ffn_post=norm_ffn_post, ffn_conv_b=ffn_conv_b)
    loss_tile, grad_x, G = _local_step(xs, target, P, late_weights, on_grad)

    small = [k for k in names if k not in big]
    packed = _pack([loss_tile[:1]] + [G[k] for k in small])
    (allsum,) = _split_start("small_grad_allsum_start",
                             [([packed], [jnp.zeros((N_DEVICES,) + packed.shape, F32)], N_DEVICES - 1, _allsum_copies)], core_arr)

    reduced, grads, deltas, new_m, new_v = {}, {}, {}, {}, {}

    def update(keys):
        for k in keys:
            gk, d, mn, vn = _adamw_call(weights[k][0], reduced[k], m_in[k][0], v_in[k][0], f"adamw_{k}")
            grads[k], deltas[k], new_m[k], new_v[k] = gk[None], d[None], mn[None], vn[None]

    others = [k for k in big if k != "w_in"]
    reduced.update(finish(others, allsum["tile"], "others"))
    update(others)
    reduced.update(finish(["w_in"], [deltas[k] for k in others], "w_in"))
    update(["w_in"])

    me_chip = jnp.stack([4 * xi + 2 * yi + ci, chip]).astype(jnp.int32)
    mine, landed = _split_wait("small_grad_allsum_wait", allsum, _allsum_copies, deltas["w_in"])
    piece_shapes = [(1, LANES)] + [(G[k].size // LANES, LANES) if k == "rel_bias" else G[k].shape for k in small]
    piece_cols = [0] + [weights[k].shape[2] if k in ("conv_dw_w", "ffn_conv_w") else 0 for k in small]
    loss_row, *summed = _ordered_sum_call(mine, landed, me_chip, piece_shapes, piece_cols)
    loss = loss_row[0, 0]
    for k, gsum in zip(small, summed):
        grads[k] = gsum.reshape(weights[k].shape)
    ds, mns, vns = _adamw_small_call([weights[k] for k in small], [grads[k] for k in small],
                                     [m_in[k] for k in small], [v_in[k] for k in small])
    deltas.update(zip(small, ds))
    new_m.update(zip(small, mns))
    new_v.update(zip(small, vns))

    return (loss, grad_x[None], *[grads[k] for k in names], *[deltas[k] for k in names],
            *[new_m[k] for k in names], *[new_v[k] for k in names])
```

```python
import functools
import math

import jax
import jax.numpy as jnp
import numpy as np
from jax import lax
from jax.experimental import pallas as pl
from jax.experimental.pallas import tpu as pltpu

F32 = jnp.float32
BF16 = jnp.bfloat16
MESH = pl.DeviceIdType.MESH

HEAD_DIM = 128
HEADS_PER_GROUP = 4
DILATED_PATTERNS = ((128, 1), (512, 4), (2048, 16))
N_GROUPS = 3
N_HEADS = N_GROUPS * HEADS_PER_GROUP
SPAN = 128
GROUP_WIDTH = HEADS_PER_GROUP * HEAD_DIM
CONV_WIDTH = 31
FFN_CONV_WIDTH = 3
N_BUCKETS = 32
MAX_DISTANCE = 2048
RMS_EPS = 1e-6
LN_EPS = 1e-5
NEG_INF = -1e30
ADAM_LR = 0.001
ADAM_B1 = 0.9
ADAM_B2 = 0.999
ADAM_EPS = 1e-08
ADAM_WD = 0.01
ADAM_STEP = 10

LANES = 128
SUBLANES = 8
PACKED_ROWS = 16
ROW_TILE = 512
GATE_ROWS, GATE_COLS = 512, 512
TIME_BLOCK = 128
CONV_PAD = 32
FFN_PAD = 8
VMEM_LIMIT = 56 << 20


def _params(sem=None, vmem=None):
    kw = {}
    if sem is not None:
        kw["dimension_semantics"] = sem
    if vmem is not None:
        kw["vmem_limit_bytes"] = vmem
    return pltpu.CompilerParams(**kw)


def _pick(n, cands):
    for c in cands:
        if n % c == 0:
            return c
    return n


ELEMENTWISE_TILE_BYTES = 3 << 19


def _row_tile(rows, cols):
    for align in (16, SUBLANES):
        fits = [t for t in range(align, rows + 1, align) if rows % t == 0 and t * cols * 4 <= ELEMENTWISE_TILE_BYTES]
        if fits:
            return max(fits)
    return SUBLANES


N_CHIPS = 4
M_TILES = (1024, 1408, 512, 256, 128)
N_TILES = (1024, 512, 1408, 256, 128)
K_TILES = (2176, 2048, 1408, 1024, 512, 256, 128)


def _matmul(a, b, mode, name, out_shards=False, tm=None):
    assert a.dtype == BF16 and b.dtype == BF16, (name, a.dtype, b.dtype)
    b3 = b.ndim == 3
    tn = tk = None
    halves = None
    if mode == "nn":
        M, K = a.shape
        N = b.shape[-1] * (N_CHIPS if b3 else 1)
        tn = b.shape[-1] if b3 else None
    elif mode == "nt":
        if a.ndim == 3:
            halves = a.shape[2]
        M, K = a.shape[-2], a.shape[-1] * (a.shape[0] if a.ndim == 3 else 1)
        N = b.shape[-2]
        tk = b.shape[-1] if b3 else None
    else:
        if b3:
            halves = b.shape[2]
        K, M = a.shape
        N = b.shape[-1] * (b.shape[0] if b3 else 1)
        tn = N // N_CHIPS if out_shards else None
    tm = tm or _pick(M, M_TILES)
    tn = tn or _pick(N, N_TILES)
    tk = tk or _pick(K, K_TILES)
    nk = K // tk
    dn = {"nn": (((1,), (0,)), ((), ())), "nt": (((1,), (1,)), ((), ())), "tn": (((0,), (0,)), ((), ()))}[mode]

    def body(a_ref, b_ref, o_ref):
        if nk == 1:
            o_ref[...] = lax.dot_general(a_ref[...], b_ref[...], dn, preferred_element_type=F32)
        else:
            @pl.when(pl.program_id(2) == 0)
            def _():
                o_ref[...] = jnp.zeros_like(o_ref)

            o_ref[...] += lax.dot_general(a_ref[...], b_ref[...], dn, preferred_element_type=F32)

    if mode == "tn":
        a_spec = pl.BlockSpec((tk, tm), lambda i, j, k: (k, i))
    elif halves:
        per = halves // tk
        a_spec = pl.BlockSpec((None, tm, tk), lambda i, j, k: (k // per, i, k % per))
    else:
        a_spec = pl.BlockSpec((tm, tk), lambda i, j, k: (i, k))
    if mode == "nn":
        b_spec = pl.BlockSpec((None, tk, tn), lambda i, j, k: (j, k, 0)) if b3 else pl.BlockSpec((tk, tn), lambda i, j, k: (k, j))
    elif mode == "nt":
        b_spec = pl.BlockSpec((None, tn, tk), lambda i, j, k: (k, j, 0)) if b3 else pl.BlockSpec((tn, tk), lambda i, j, k: (j, k))
    elif halves:
        per = halves // tn
        b_spec = pl.BlockSpec((None, tk, tn), lambda i, j, k: (j // per, k, j % per))
    else:
        b_spec = pl.BlockSpec((tk, tn), lambda i, j, k: (k, j))
    if out_shards:
        out_spec = pl.BlockSpec((None, tm, tn), lambda i, j, k: (j, i, 0))
        out_shape = jax.ShapeDtypeStruct((N_CHIPS, M, tn), F32)
    else:
        out_spec = pl.BlockSpec((tm, tn), lambda i, j, k: (i, j))
        out_shape = jax.ShapeDtypeStruct((M, N), F32)
    return pl.pallas_call(
        body, name=name, grid=(M // tm, N // tn, nk),
        in_specs=[a_spec, b_spec], out_specs=out_spec, out_shape=out_shape,
        compiler_params=_params(("parallel", "parallel", "arbitrary"), VMEM_LIMIT),
    )(a, b)


def _grad_half_matmul(a, name, chip_half, b, init=None):
    K, M = a.shape
    parts = b.ndim == 3
    N = b.shape[-1] * (b.shape[0] if parts else 1)
    h, tn = M // 2, N // N_CHIPS
    summed = init is not None
    dn = (((0,), (0,)), ((), ()))

    def body(ch_ref, a_ref, b_ref, *rest):
        product = lax.dot_general(a_ref[...], b_ref[...], dn, preferred_element_type=F32)
        if not summed:
            rest[0][...] = product
            return
        init_ref, own_ref, sum16_ref = rest
        total = product + init_ref[...]
        sum16_ref[...] = total.astype(BF16)

        @pl.when(pl.program_id(0) == ch_ref[0])
        def _():
            own_ref[...] = total

    if parts:
        per = b.shape[2] // tn
        b_spec = pl.BlockSpec((None, K, tn), lambda j, ch_ref: (j // per, 0, j % per))
    else:
        b_spec = pl.BlockSpec((K, tn), lambda j, ch_ref: (0, j))
    shard_spec = pl.BlockSpec((None, h, tn), lambda j, ch_ref: (j, 0, 0))
    own_spec = pl.BlockSpec((h, tn), lambda j, ch_ref: (0, 0))
    shape = (N_CHIPS, h, tn)
    return pl.pallas_call(
        body, name=name + ("_mine" if summed else "_theirs"),
        grid_spec=pltpu.PrefetchScalarGridSpec(
            num_scalar_prefetch=1, grid=(N_CHIPS,),
            in_specs=[pl.BlockSpec((K, h), lambda j, ch_ref: (0, ch_ref[1])), b_spec] + [shard_spec] * summed,
            out_specs=[own_spec, shard_spec] if summed else shard_spec),
        out_shape=[jax.ShapeDtypeStruct((h, tn), F32), jax.ShapeDtypeStruct(shape, BF16)] if summed
        else jax.ShapeDtypeStruct(shape, F32),
        compiler_params=_params(("arbitrary",), VMEM_LIMIT),
    )(chip_half, a, b, *([init] if summed else []))


def _rms(x, g):
    r = lax.rsqrt(jnp.mean(x * x, axis=-1, keepdims=True) + RMS_EPS)
    return x * r * g


def _rms_bwd(x, g, dy):
    r = lax.rsqrt(jnp.mean(x * x, axis=-1, keepdims=True) + RMS_EPS)
    n = x * r
    dn = dy * g
    dx = r * (dn - n * jnp.mean(dn * n, axis=-1, keepdims=True))
    return dx, jnp.sum(dy * n, axis=0, keepdims=True)


def _sigmoid(x):
    return 1.0 / (1.0 + jnp.exp(-x))


_GELU_C = math.sqrt(2.0 / math.pi)


def _gelu(x):
    return 0.5 * x * (1.0 + jnp.tanh(_GELU_C * (x + 0.044715 * x * x * x)))


def _gelu_and_grad(x):
    x2 = x * x
    t = jnp.tanh(_GELU_C * x * (1.0 + 0.044715 * x2))
    half = 0.5 * (1.0 + t)
    return x * half, half + (0.5 * _GELU_C) * x * (1.0 - t * t) * (1.0 + (3.0 * 0.044715) * x2)


def _row_spec(width, col_block=0):
    return pl.BlockSpec((ROW_TILE, width), lambda i: (i, col_block))


def _vec_spec(width, col_block=0):
    return pl.BlockSpec((1, width), lambda i: (0, col_block))


def _accumulate(ref, part):
    @pl.when(pl.program_id(0) == 0)
    def _():
        ref[...] = part

    @pl.when(pl.program_id(0) > 0)
    def _():
        ref[...] += part


def _rms_fwd_call(x, g):
    S, D = x.shape

    def body(x_ref, g_ref, h_ref):
        h_ref[...] = _rms(x_ref[...], g_ref[...]).astype(BF16)

    return pl.pallas_call(
        body, name="rms_mix_pre", grid=(S // ROW_TILE,),
        in_specs=[_row_spec(D), _vec_spec(D)], out_specs=_row_spec(D),
        out_shape=jax.ShapeDtypeStruct((S, D), BF16),
        compiler_params=_params(("parallel",)),
    )(x, g)


def _ln_silu_call(c1, g, b):
    S, C = c1.shape

    def body(c_ref, g_ref, b_ref, o_ref):
        xv = c_ref[...]
        mu = jnp.mean(xv, axis=-1, keepdims=True)
        xc = xv - mu
        var = jnp.mean(xc * xc, axis=-1, keepdims=True)
        z = xc * lax.rsqrt(var + LN_EPS) * g_ref[...] + b_ref[...]
        o_ref[...] = (z * _sigmoid(z)).astype(BF16)

    return pl.pallas_call(
        body, name="conv_ln_silu", grid=(S // ROW_TILE,),
        in_specs=[_row_spec(C), _vec_spec(C), _vec_spec(C)], out_specs=_row_spec(C),
        out_shape=jax.ShapeDtypeStruct((S, C), BF16),
        compiler_params=_params(("parallel",)),
    )(c1, g, b)


def _ln_silu_bwd_call(c1, g, b, dc):
    S, C = c1.shape

    def body(c_ref, g_ref, b_ref, dc_ref, dx_ref, dg_ref, db_ref):
        xv = c_ref[...]
        mu = jnp.mean(xv, axis=-1, keepdims=True)
        xc = xv - mu
        rs = lax.rsqrt(jnp.mean(xc * xc, axis=-1, keepdims=True) + LN_EPS)
        xh = xc * rs
        z = xh * g_ref[...] + b_ref[...]
        sg = _sigmoid(z)
        dz = dc_ref[...] * (sg * (1.0 + z * (1.0 - sg)))
        dxh = dz * g_ref[...]
        dx_ref[...] = rs * (dxh - jnp.mean(dxh, axis=-1, keepdims=True) - xh * jnp.mean(dxh * xh, axis=-1, keepdims=True))
        _accumulate(dg_ref, jnp.sum(dz * xh, axis=0, keepdims=True))
        _accumulate(db_ref, jnp.sum(dz, axis=0, keepdims=True))

    return pl.pallas_call(
        body, name="conv_ln_silu_bwd", grid=(S // ROW_TILE,),
        in_specs=[_row_spec(C), _vec_spec(C), _vec_spec(C), _row_spec(C)],
        out_specs=[_row_spec(C), _vec_spec(C), _vec_spec(C)],
        out_shape=[jax.ShapeDtypeStruct((S, C), F32), jax.ShapeDtypeStruct((1, C), F32), jax.ShapeDtypeStruct((1, C), F32)],
        compiler_params=_params(("arbitrary",)),
    )(c1, g, b, dc)


def _mix_call(proj, gate_col0, b_gate, y_a, y_c):
    S, D = y_a.shape
    w = GATE_COLS
    nc = D // w
    ga0, gc0 = gate_col0 // w, (gate_col0 + D) // w

    def body(ga_ref, gc_ref, ba_ref, bc_ref, ya_ref, yc_ref, o_ref):
        o_ref[...] = (_sigmoid(ga_ref[...] + ba_ref[...]) * ya_ref[...]
                      + _sigmoid(gc_ref[...] + bc_ref[...]) * yc_ref[...]).astype(BF16)

    tile = lambda off: pl.BlockSpec((GATE_ROWS, w), lambda i, j: (i, off + j))
    vec = lambda off: pl.BlockSpec((1, w), lambda i, j: (0, off + j))
    return pl.pallas_call(
        body, name="gate_mix", grid=(S // GATE_ROWS, nc),
        in_specs=[tile(ga0), tile(gc0), vec(0), vec(nc), tile(0), tile(0)],
        out_specs=tile(0), out_shape=jax.ShapeDtypeStruct((S, D), BF16),
        compiler_params=_params(("parallel", "parallel")),
    )(proj, proj, b_gate, b_gate, y_a, y_c)


def _window_stores(stage_ref, slot, dst_ref, rows, cols, sems):
    width = stage_ref.shape[-1]
    return [pltpu.make_async_copy(stage_ref.at[slot, p], dst_ref.at[rows, pl.ds(pl.multiple_of(c, LANES), width)], sems.at[slot, p])
            for p, c in enumerate(cols)]


def _staged_window_stores(stage_ref, dst_ref, sems, step, n_steps, rows, cols, fill):
    slot = step % 2
    copies = lambda s: _window_stores(stage_ref, s, dst_ref, rows, cols, sems)

    @pl.when(step >= 2)
    def _():
        for cp in copies(slot):
            cp.wait()

    fill(slot)
    for cp in copies(slot):
        cp.start()

    @pl.when(step == n_steps - 1)
    def _():
        for s in ([slot, 1 - slot] if n_steps > 1 else [slot]):
            for cp in copies(s):
                cp.wait()


def _mix_bwd_call(dmixed, proj, gate_col0, b_gate, y_a, y_c):
    S, D = y_a.shape
    w = GATE_COLS
    nc = D // w
    nr = S // GATE_ROWS
    ga0, gc0 = gate_col0 // w, (gate_col0 + D) // w

    def body(dm_ref, ga_ref, gc_ref, ba_ref, bc_ref, ya_ref, yc_ref, dya_ref, dyc_ref, dproj_ref, dba_ref, dbc_ref,
             stage_ref, sems):
        j, i = pl.program_id(0), pl.program_id(1)
        dm = dm_ref[...]
        sa = _sigmoid(ga_ref[...] + ba_ref[...])
        sc = _sigmoid(gc_ref[...] + bc_ref[...])
        dya_ref[...] = (dm * sa).astype(BF16)
        dyc_ref[...] = (dm * sc).astype(BF16)
        dga = dm * ya_ref[...] * sa * (1.0 - sa)
        dgc = dm * yc_ref[...] * sc * (1.0 - sc)

        def fill(slot):
            stage_ref[slot, 0] = dga.astype(BF16)
            stage_ref[slot, 1] = dgc.astype(BF16)

        rows = pl.ds(pl.multiple_of(i * GATE_ROWS, GATE_ROWS), GATE_ROWS)
        _staged_window_stores(stage_ref, dproj_ref, sems, j * nr + i, nc * nr, rows,
                              [gate_col0 + j * w, gate_col0 + D + j * w], fill)
        pa = jnp.sum(dga, axis=0, keepdims=True)
        pc = jnp.sum(dgc, axis=0, keepdims=True)

        @pl.when(i == 0)
        def _():
            dba_ref[...] = pa
            dbc_ref[...] = pc

        @pl.when(i > 0)
        def _():
            dba_ref[...] += pa
            dbc_ref[...] += pc

    tile = lambda off: pl.BlockSpec((GATE_ROWS, w), lambda j, i: (i, off + j))
    vec = lambda off: pl.BlockSpec((1, w), lambda j, i: (0, off + j))
    return pl.pallas_call(
        body, name="gate_mix_bwd", grid=(nc, nr),
        in_specs=[tile(0), tile(ga0), tile(gc0), vec(0), vec(nc), tile(0), tile(0)],
        out_specs=[tile(0), tile(0), ANY, vec(0), vec(0)],
        out_shape=[jax.ShapeDtypeStruct((S, D), BF16)] * 2 + [jax.ShapeDtypeStruct((S, proj.shape[1]), BF16)] + [
                   jax.ShapeDtypeStruct((1, D), F32), jax.ShapeDtypeStruct((1, D), F32)],
        scratch_shapes=[pltpu.VMEM((2, 2, GATE_ROWS, w), BF16), pltpu.SemaphoreType.DMA((2, 2))],
        compiler_params=_params(("arbitrary", "arbitrary")),
    )(dmixed, proj, proj, b_gate, b_gate, y_a, y_c)


def _res1_call(x, out, g_post, g_pre):
    S, D = x.shape

    def body(x_ref, o_ref, gp_ref, gq_ref, x1_ref, h2_ref):
        x1 = x_ref[...] + _rms(o_ref[...], gp_ref[...])
        x1_ref[...] = x1
        h2_ref[...] = _rms(x1, gq_ref[...]).astype(BF16)

    return pl.pallas_call(
        body, name="residual_mix", grid=(S // ROW_TILE,),
        in_specs=[_row_spec(D), _row_spec(D), _vec_spec(D), _vec_spec(D)],
        out_specs=[_row_spec(D), _row_spec(D)],
        out_shape=[jax.ShapeDtypeStruct((S, D), F32), jax.ShapeDtypeStruct((S, D), BF16)],
        compiler_params=_params(("parallel",)),
    )(x, out, g_post, g_pre)


def _loss_call(y, x1, g_post, target):
    S, D = y.shape

    def body(y_ref, x1_ref, g_ref, t_ref, loss_ref, dx_ref, dy_ref, dg_ref):
        yv, gv = y_ref[...], g_ref[...]
        err = x1_ref[...] + _rms(yv, gv) - t_ref[...]
        dx2 = err * (1.0 / D)
        dx_ref[...] = dx2
        dy, dg = _rms_bwd(yv, gv, dx2)
        dy_ref[...] = dy.astype(BF16)
        _accumulate(dg_ref, dg)
        part = 0.5 * jnp.sum(jnp.mean(err * err, axis=-1, keepdims=True), axis=0, keepdims=True)
        _accumulate(loss_ref, jnp.broadcast_to(part, (SUBLANES, LANES)))

    return pl.pallas_call(
        body, name="residual_ffn_loss", grid=(S // ROW_TILE,),
        in_specs=[_row_spec(D), _row_spec(D), _vec_spec(D), _row_spec(D)],
        out_specs=[pl.BlockSpec((SUBLANES, LANES), lambda i: (0, 0)), _row_spec(D), _row_spec(D), _vec_spec(D)],
        out_shape=[jax.ShapeDtypeStruct((SUBLANES, LANES), F32), jax.ShapeDtypeStruct((S, D), F32),
                   jax.ShapeDtypeStruct((S, D), BF16), jax.ShapeDtypeStruct((1, D), F32)],
        compiler_params=_params(("arbitrary",)),
    )(y, x1, g_post, target)


def _mid_bwd_call(x1, g_pre, dh2, dx2, out, g_post):
    S, D = x1.shape

    def body(x1_ref, gq_ref, dh_ref, dx2_ref, o_ref, gp_ref, dx1_ref, do_ref, dgq_ref, dgp_ref):
        d, dgq = _rms_bwd(x1_ref[...], gq_ref[...], dh_ref[...])
        dx1 = dx2_ref[...] + d
        dx1_ref[...] = dx1
        do, dgp = _rms_bwd(o_ref[...], gp_ref[...], dx1)
        do_ref[...] = do.astype(BF16)
        _accumulate(dgq_ref, dgq)
        _accumulate(dgp_ref, dgp)

    return pl.pallas_call(
        body, name="residual_mix_bwd", grid=(S // ROW_TILE,),
        in_specs=[_row_spec(D), _vec_spec(D), _row_spec(D), _row_spec(D), _row_spec(D), _vec_spec(D)],
        out_specs=[_row_spec(D), _row_spec(D), _vec_spec(D), _vec_spec(D)],
        out_shape=[jax.ShapeDtypeStruct((S, D), F32), jax.ShapeDtypeStruct((S, D), BF16)] + [jax.ShapeDtypeStruct((1, D), F32)] * 2,
        compiler_params=_params(("arbitrary",)),
    )(x1, g_pre, dh2, dx2, out, g_post)


def _in_bwd_call(x, g, dh1, dx1):
    S, D = x.shape

    def body(x_ref, g_ref, dh_ref, dx1_ref, gx_ref, dg_ref):
        d, dg = _rms_bwd(x_ref[...], g_ref[...], dh_ref[...])
        gx_ref[...] = dx1_ref[...] + d
        _accumulate(dg_ref, dg)

    return pl.pallas_call(
        body, name="rms_mix_pre_bwd", grid=(S // ROW_TILE,),
        in_specs=[_row_spec(D), _vec_spec(D), _row_spec(D), _row_spec(D)],
        out_specs=[_row_spec(D), _vec_spec(D)],
        out_shape=[jax.ShapeDtypeStruct((S, D), F32), jax.ShapeDtypeStruct((1, D), F32)],
        compiler_params=_params(("arbitrary",)),
    )(x, g, dh1, dx1)


def _bucket_table(dilation):
    qi = np.arange(SPAN)[:, None]
    ki = np.arange(2 * SPAN)[None, :]
    dist = np.maximum(qi + SPAN - ki, 0) * dilation
    max_exact = N_BUCKETS // 2
    d = np.maximum(dist, 1).astype(np.float64)
    large = max_exact + (np.log(d / max_exact) / math.log(MAX_DISTANCE / max_exact) * (N_BUCKETS - max_exact)).astype(np.int32)
    large = np.minimum(large, N_BUCKETS - 1)
    return np.where(dist < max_exact, dist, large).astype(np.int32)


def _bucket_tables():
    return jnp.asarray(np.stack([_bucket_table(r) for _, r in DILATED_PATTERNS]))


def _bias_table_call(rel_bias, buckets):
    def body(rb_ref, bk_ref, o_ref):
        for h in range(N_HEADS):
            bk = bk_ref[h // HEADS_PER_GROUP]

            def step(b, acc):
                return jnp.where(bk == b, rb_ref[b, h], acc)

            o_ref[h] = lax.fori_loop(0, N_BUCKETS, step, jnp.zeros((SPAN, 2 * SPAN), F32))

    return pl.pallas_call(
        body, name="rel_bias_table",
        in_specs=[pl.BlockSpec(memory_space=pltpu.SMEM), pl.BlockSpec(memory_space=pltpu.VMEM)],
        out_specs=pl.BlockSpec(memory_space=pltpu.VMEM),
        out_shape=jax.ShapeDtypeStruct((N_HEADS, SPAN, 2 * SPAN), F32),
    )(rel_bias, buckets)


def _bias_grad_call(dbias, buckets):
    def body(db_ref, bk_ref, o_ref, rows_ref):
        for h in range(N_HEADS):
            bk = bk_ref[h // HEADS_PER_GROUP]
            dv = db_ref[h]

            def step(b, carry):
                rows_ref[h, b] = jnp.sum(jnp.where(bk == b, dv, 0.0), axis=0, keepdims=True)
                return carry

            lax.fori_loop(0, N_BUCKETS, step, 0)
        o_ref[...] = jnp.sum(rows_ref[...], axis=-1, keepdims=True)

    out = pl.pallas_call(
        body, name="rel_bias_grad",
        in_specs=[pl.BlockSpec(memory_space=pltpu.VMEM), pl.BlockSpec(memory_space=pltpu.VMEM)],
        out_specs=pl.BlockSpec(memory_space=pltpu.VMEM),
        out_shape=jax.ShapeDtypeStruct((N_HEADS, N_BUCKETS, 1, 1), F32),
        scratch_shapes=[pltpu.VMEM((N_HEADS, N_BUCKETS, 1, 2 * SPAN), F32)],
    )(dbias, buckets)
    return out.reshape(N_HEADS, N_BUCKETS).T


def _dot_nt(a, b):
    return lax.dot_general(a, b, (((1,), (1,)), ((), ())), preferred_element_type=F32)


def _dot_nn(a, b):
    return lax.dot_general(a, b, (((1,), (0,)), ((), ())), preferred_element_type=F32)


def _dot_tn(a, b):
    return lax.dot_general(a, b, (((0,), (0,)), ((), ())), preferred_element_type=F32)


def _band_masks(n, nb):
    qi = lax.broadcasted_iota(jnp.int32, (SPAN, SPAN), 0)
    ki = lax.broadcasted_iota(jnp.int32, (SPAN, SPAN), 1)
    prev_ok = jnp.logical_and(ki >= qi, n > 0)
    cur_ok = ki <= qi
    next_ok = jnp.logical_and(ki >= qi, n < nb - 1)
    return prev_ok, cur_ok, next_ok


def _wide_band_mask(n):
    qi = lax.broadcasted_iota(jnp.int32, (SPAN, 2 * SPAN), 0)
    ki = lax.broadcasted_iota(jnp.int32, (SPAN, 2 * SPAN), 1)
    prev_ok = jnp.logical_and(jnp.logical_and(ki < SPAN, ki >= qi), n > 0)
    cur_ok = jnp.logical_and(ki >= SPAN, ki - SPAN <= qi)
    return jnp.logical_or(prev_ok, cur_ok)


def _attn_plan(S, group):
    r = DILATED_PATTERNS[group][1]
    hp, per = (HEADS_PER_GROUP, 1) if r == 1 else (2, 4)
    return r, S // (r * SPAN), hp, per


def _residue_rows(rho, r):
    return slice(None) if r == 1 else pl.ds(rho, SPAN, stride=r)


def _for_residues(r, per, fn):
    if r == per:
        for u in range(per):
            fn(u)
        return

    def step(i, carry):
        for u in range(per):
            fn(i * per + u)
        return carry

    lax.fori_loop(0, r // per, step, 0)


def _attn_fwd_call(proj, bias, group):
    S = proj.shape[0]
    r, nb, hp, per = _attn_plan(S, group)
    scale = HEAD_DIM ** -0.5
    kinds = ("q", "kp", "kc", "vp", "vc") if nb > 1 else ("q", "kc", "vc")

    per_kind = _refs_per_kind(r, hp)

    def body(*refs):
        ins = {kind: refs[i * per_kind:(i + 1) * per_kind] for i, kind in enumerate(kinds)}
        b_ref, o_ref, lse_ref = refs[len(kinds) * per_kind:]
        n = pl.program_id(1)
        prev_ok, cur_ok, _ = _band_masks(n, nb)

        band_ok = _wide_band_mask(n) if nb > 1 else cur_ok

        def residue(rho):
            rows = _residue_rows(rho, r)
            for j in range(hp):
                get = lambda kind: _head_rows(ins[kind], j, rows, r).astype(BF16)
                q = get("q")
                if nb > 1:
                    keys, vals, bias_j = jnp.concatenate([get("kp"), get("kc")], axis=0), jnp.concatenate([get("vp"), get("vc")], axis=0), b_ref[j]
                else:
                    keys, vals, bias_j = get("kc"), get("vc"), b_ref[j, :, SPAN:]
                s = jnp.where(band_ok, _dot_nt(q, keys) * scale + bias_j, NEG_INF)
                m = jnp.max(s, axis=-1, keepdims=True)
                p = jnp.exp(s - m)
                den = jnp.sum(p, axis=-1, keepdims=True)
                o_ref[j, rows, :] = _dot_nn(p.astype(BF16), vals) / den
                lse_ref[j, rows, :] = jnp.broadcast_to(m + jnp.log(den), (SPAN, HEAD_DIM))

        _for_residues(r, per, residue)

    in_specs = [_head_spec(r, nb, hp, kind, group, jj) for kind in kinds for jj in range(per_kind)]
    in_specs.append(pl.BlockSpec((hp, SPAN, 2 * SPAN), lambda j, n: (group * (HEADS_PER_GROUP // hp) + j, 0, 0)))
    out = pl.BlockSpec((hp, r * SPAN, HEAD_DIM), lambda j, n: (j, n, 0))
    return pl.pallas_call(
        body, name=f"attn_fwd_g{group}", grid=(HEADS_PER_GROUP // hp, nb),
        in_specs=in_specs, out_specs=[out] * 2,
        out_shape=[jax.ShapeDtypeStruct((HEADS_PER_GROUP, S, HEAD_DIM), F32)] * 2,
        compiler_params=_params(("parallel", "parallel"), VMEM_LIMIT),
    )(*([proj] * (len(in_specs) - 1)), bias)


_PROJ_PART = dict(q=0, qn=0, kp=1, kc=1, vp=2, vc=2)


def _refs_per_kind(r, hp):
    return 1 if r == 1 else hp


def _head_rows(refs, j, rows, r):
    return refs[0][:, j * HEAD_DIM:(j + 1) * HEAD_DIM] if r == 1 else refs[j][rows, :]


def _head_spec(r, nb, hp, kind, group, jj):
    if kind in _PROJ_PART:
        base = (_PROJ_PART[kind] * N_GROUPS + group) * HEADS_PER_GROUP
    else:
        base = 0
    if kind.endswith("p"):
        row = lambda n: jnp.maximum(n - 1, 0)
    elif kind.endswith("n"):
        row = lambda n: jnp.minimum(n + 1, nb - 1)
    else:
        row = lambda n: n
    if r == 1:
        return pl.BlockSpec((SPAN, hp * HEAD_DIM), lambda j, n: (row(n), base // hp + j))
    return pl.BlockSpec((r * SPAN, HEAD_DIM), lambda j, n: (row(n), base + j * hp + jj))


def _attn_merge_call(parts):
    S = parts[0].shape[1]

    def body(o1, s1, o2, s2, o3, s3, a_ref, ab_ref, lse_ref):
        for j in range(HEADS_PER_GROUP):
            sl = slice(j * HEAD_DIM, (j + 1) * HEAD_DIM)
            mx = jnp.maximum(jnp.maximum(s1[j], s2[j]), s3[j])
            w1 = jnp.exp(s1[j] - mx)
            w2 = jnp.exp(s2[j] - mx)
            w3 = jnp.exp(s3[j] - mx)
            den = w1 + w2 + w3
            a = (w1 * o1[j] + w2 * o2[j] + w3 * o3[j]) / den
            a_ref[:, sl] = a
            ab_ref[:, sl] = a.astype(BF16)
            lse_ref[:, sl] = mx + jnp.log(den)

    heads = pl.BlockSpec((HEADS_PER_GROUP, ROW_TILE, HEAD_DIM), lambda i: (0, i, 0))
    return pl.pallas_call(
        body, name="attn_merge", grid=(S // ROW_TILE,),
        in_specs=[heads] * 6, out_specs=[_row_spec(GROUP_WIDTH)] * 3,
        out_shape=[jax.ShapeDtypeStruct((S, GROUP_WIDTH), F32), jax.ShapeDtypeStruct((S, GROUP_WIDTH), BF16),
                   jax.ShapeDtypeStruct((S, GROUP_WIDTH), F32)],
        compiler_params=_params(("parallel",)),
    )(*parts)


def _attn_delta_call(a, da):
    S = a.shape[0]

    def body(a_ref, da_ref, d_ref):
        for j in range(HEADS_PER_GROUP):
            sl = slice(j * HEAD_DIM, (j + 1) * HEAD_DIM)
            d = jnp.sum(a_ref[:, sl] * da_ref[:, sl], axis=-1, keepdims=True)
            d_ref[:, sl] = jnp.broadcast_to(d, (ROW_TILE, HEAD_DIM))

    return pl.pallas_call(
        body, name="attn_delta", grid=(S // ROW_TILE,),
        in_specs=[_row_spec(GROUP_WIDTH)] * 2, out_specs=_row_spec(GROUP_WIDTH),
        out_shape=jax.ShapeDtypeStruct((S, GROUP_WIDTH), F32),
        compiler_params=_params(("parallel",)),
    )(a, da)


def _attn_bwd_call(proj, bias, da, lse, delta, group, dproj):
    S = proj.shape[0]
    r, nb, hp, per = _attn_plan(S, group)
    scale = HEAD_DIM ** -0.5
    kinds = ("q", "qn", "kp", "kc", "vp", "vc", "da", "dan", "lse", "lsen", "dl", "dln") if nb > 1 else ("q", "kc", "vc", "da", "lse", "dl")
    source = dict(da=da, dan=da, lse=lse, lsen=lse, dl=delta, dln=delta)

    per_kind = _refs_per_kind(r, hp)
    per_group = HEADS_PER_GROUP // hp
    block_rows = r * SPAN

    def body(*refs):
        ins = {kind: refs[i * per_kind:(i + 1) * per_kind] for i, kind in enumerate(kinds)}
        b_ref, _, dproj_ref, db_ref, stage_ref, sems = refs[len(kinds) * per_kind:][:6]
        strided_ref = None if r == 1 else refs[-1]
        jg, n = pl.program_id(0), pl.program_id(1)
        prev_ok, cur_ok, next_ok = _band_masks(n, nb)

        @pl.when(n == 0)
        def _():
            db_ref[...] = jnp.zeros_like(db_ref)

        band_ok = _wide_band_mask(n) if nb > 1 else cur_ok

        def fill(slot):
            def put(part, j, rows, value):
                if r == 1:
                    stage_ref[slot, part, :, j * HEAD_DIM:(j + 1) * HEAD_DIM] = value.astype(BF16)
                else:
                    strided_ref[part, j, rows, :] = value

            _for_residues(r, per, functools.partial(residue, put))
            if r > 1:
                for part in range(3):
                    for j in range(hp):
                        for t0 in range(0, block_rows, ROW_TILE):
                            stage_ref[slot, part, t0:t0 + ROW_TILE, j * HEAD_DIM:(j + 1) * HEAD_DIM] = (
                                strided_ref[part, j, t0:t0 + ROW_TILE, :].astype(BF16))

        def residue(put, rho):
            rows = _residue_rows(rho, r)
            for j in range(hp):
                get = lambda kind: _head_rows(ins[kind], j, rows, r)
                q = get("q").astype(BF16)
                kc = get("kc").astype(BF16)
                vc = get("vc").astype(BF16)
                dav = get("da").astype(BF16)
                lse_q, dl_q = get("lse"), get("dl")
                if nb == 1:
                    pc = jnp.exp(jnp.where(cur_ok, _dot_nt(q, kc) * scale + b_ref[j, :, SPAN:], NEG_INF) - lse_q)
                    dsc = pc * (_dot_nt(dav, vc) - dl_q)
                    dsc_b = dsc.astype(BF16)
                    dq = _dot_nn(dsc_b, kc)
                    dk = _dot_tn(dsc_b, q)
                    dv = _dot_tn(pc.astype(BF16), dav)
                    db_ref[j, :, SPAN:] += dsc
                else:
                    qn = get("qn").astype(BF16)
                    dan = get("dan").astype(BF16)
                    keys = jnp.concatenate([get("kp").astype(BF16), kc], axis=0)
                    vals = jnp.concatenate([get("vp").astype(BF16), vc], axis=0)
                    wide = lambda t: jnp.concatenate([t, t], axis=1)
                    p = jnp.exp(jnp.where(band_ok, _dot_nt(q, keys) * scale + b_ref[j], NEG_INF) - wide(lse_q))
                    ds = p * (_dot_nt(dav, vals) - wide(dl_q))
                    dq = _dot_nn(ds.astype(BF16), keys)
                    db_ref[j] += ds
                    pn = jnp.exp(jnp.where(next_ok, _dot_nt(qn, kc) * scale + b_ref[j, :, :SPAN], NEG_INF) - get("lsen"))
                    dsn = pn * (_dot_nt(dan, vc) - get("dln"))
                    both = lambda cur_part, next_part: jnp.concatenate([cur_part.astype(BF16), next_part.astype(BF16)], axis=0)
                    dk = _dot_tn(both(ds[:, SPAN:], dsn), jnp.concatenate([q, qn], axis=0))
                    dv = _dot_tn(both(p[:, SPAN:], pn), jnp.concatenate([dav, dan], axis=0))
                put(0, j, rows, dq * scale)
                put(1, j, rows, dk * scale)
                put(2, j, rows, dv)

        cols = [(part * N_GROUPS + group) * GROUP_WIDTH + jg * (hp * HEAD_DIM) for part in range(3)]
        rows = pl.ds(pl.multiple_of(n * block_rows, SPAN), block_rows)
        _staged_window_stores(stage_ref, dproj_ref, sems, jg * nb + n, per_group * nb, rows, cols, fill)

    band = (hp, SPAN, 2 * SPAN)
    in_specs = [_head_spec(r, nb, hp, kind, group, jj) for kind in kinds for jj in range(per_kind)]
    in_specs += [pl.BlockSpec(band, lambda j, n: (group * per_group + j, 0, 0)), ANY]
    operands = [source.get(kind, proj) for kind in kinds for _ in range(per_kind)] + [bias, dproj]
    scratch = [pltpu.VMEM((2, 3, block_rows, hp * HEAD_DIM), BF16), pltpu.SemaphoreType.DMA((2, 3))]
    if r > 1:
        scratch.append(pltpu.VMEM((3, hp, block_rows, HEAD_DIM), F32))
    return pl.pallas_call(
        body, name=f"attn_bwd_g{group}", grid=(per_group, nb),
        in_specs=in_specs,
        out_specs=[ANY, pl.BlockSpec(band, lambda j, n: (j, 0, 0))],
        out_shape=[jax.ShapeDtypeStruct(dproj.shape, BF16), jax.ShapeDtypeStruct((HEADS_PER_GROUP, SPAN, 2 * SPAN), F32)],
        input_output_aliases={len(operands) - 1: 0},
        scratch_shapes=scratch,
        compiler_params=_params(("arbitrary", "arbitrary"), VMEM_LIMIT),
    )(*operands)


def _tap_rows(xpad_ref, t0, k, width, pad):
    return xpad_ref[pl.ds(t0 + (pad - (width - 1 - k)), TIME_BLOCK), :]


def _conv_block(xpad_ref, t0, w_ref, width, pad):
    acc = None
    for k in range(width):
        term = w_ref[k:k + 1, :] * _tap_rows(xpad_ref, t0, k, width, pad)
        acc = term if acc is None else acc + term
    return acc


def _conv_transpose_block(dpad_ref, t0, w_ref, width):
    acc = None
    for k in range(width):
        term = w_ref[k:k + 1, :] * dpad_ref[pl.ds(t0 + (width - 1 - k), TIME_BLOCK), :]
        acc = term if acc is None else acc + term
    return acc


def _conv_weight_grad(xpad_ref, t0, dy, dw_ref, width, pad):
    for k in range(width):
        dw_ref[k:k + 1, :] += jnp.sum(dy * _tap_rows(xpad_ref, t0, k, width, pad), axis=0, keepdims=True)


def _time_loop(S, step, skip_first=0, skip_last=0):
    def it(tb, carry):
        step(pl.multiple_of(tb * TIME_BLOCK, TIME_BLOCK))
        return carry

    lax.fori_loop(skip_first, S // TIME_BLOCK - skip_last, it, 0)


def _fill_head(head_ref, x_ref, pad):
    head_ref[0:pad, :] = jnp.zeros((pad, LANES), F32)
    head_ref[pad:, :] = x_ref[0:TIME_BLOCK, :]


def _fill_tail(tail_ref, x_ref, pad):
    S = x_ref.shape[0]
    tail_ref[0:TIME_BLOCK, :] = x_ref[S - TIME_BLOCK:S, :]
    tail_ref[TIME_BLOCK:, :] = jnp.zeros((pad, LANES), F32)


def _conv_fwd_call(proj, col0, w, b):
    S = proj.shape[0]
    C = w.shape[1]
    nt = C // LANES
    v0, g0 = col0 // LANES, (col0 + C) // LANES

    def body(val_ref, gate_ref, w_ref, b_ref, o_ref, pad_ref):
        pad_ref[0:CONV_PAD, :] = jnp.zeros((CONV_PAD, LANES), F32)
        pad_ref[CONV_PAD:, :] = val_ref[...] * _sigmoid(gate_ref[...])

        def step(t0):
            o_ref[pl.ds(t0, TIME_BLOCK), :] = _conv_block(pad_ref, t0, w_ref, CONV_WIDTH, CONV_PAD) + b_ref[...]

        _time_loop(S, step)

    seq = lambda off: pl.BlockSpec((S, LANES), lambda i: (0, off + i))
    return pl.pallas_call(
        body, name="conv_module", grid=(nt,),
        in_specs=[seq(v0), seq(g0), pl.BlockSpec((CONV_WIDTH, LANES), lambda i: (0, i)), pl.BlockSpec((1, LANES), lambda i: (0, i))],
        out_specs=seq(0), out_shape=jax.ShapeDtypeStruct((S, C), F32),
        scratch_shapes=[pltpu.VMEM((S + CONV_PAD, LANES), F32)],
        compiler_params=_params(("parallel",)),
    )(proj, proj, w, b)


def _conv_bwd_call(proj, col0, w, dc1, dproj):
    S = proj.shape[0]
    C = w.shape[1]
    nt = C // LANES
    v0, g0 = col0 // LANES, (col0 + C) // LANES

    def body(val_ref, gate_ref, w_ref, dy_ref, _, dproj_ref, dw_ref, db_ref, xpad_ref, tail_ref, dwacc_ref, stage_ref, sems):
        i = pl.program_id(0)
        xpad_ref[0:CONV_PAD, :] = jnp.zeros((CONV_PAD, LANES), F32)
        xpad_ref[CONV_PAD:, :] = val_ref[...] * _sigmoid(gate_ref[...])
        _fill_tail(tail_ref, dy_ref, CONV_PAD)
        dwacc_ref[...] = jnp.zeros_like(dwacc_ref)

        def fill(slot):
            def block(t0, dy_src, dy_t0):
                rows = pl.ds(t0, TIME_BLOCK)
                _conv_weight_grad(xpad_ref, t0, dy_ref[rows, :], dwacc_ref, CONV_WIDTH, CONV_PAD)
                dc0 = _conv_transpose_block(dy_src, dy_t0, w_ref, CONV_WIDTH)
                sg = _sigmoid(gate_ref[rows, :])
                stage_ref[slot, 0, rows, :] = (dc0 * sg).astype(BF16)
                stage_ref[slot, 1, rows, :] = (dc0 * val_ref[rows, :] * sg * (1.0 - sg)).astype(BF16)

            _time_loop(S, lambda t0: block(t0, dy_ref, t0), skip_last=1)
            block(S - TIME_BLOCK, tail_ref, 0)

        _staged_window_stores(stage_ref, dproj_ref, sems, i, nt, pl.ds(0, S),
                              [col0 + i * LANES, col0 + C + i * LANES], fill)
        dw_ref[...] = dwacc_ref[...]
        db_ref[...] = jnp.sum(dy_ref[...], axis=0, keepdims=True)

    seq = lambda off: pl.BlockSpec((S, LANES), lambda i: (0, off + i))
    return pl.pallas_call(
        body, name="conv_module_bwd", grid=(nt,),
        in_specs=[seq(v0), seq(g0), pl.BlockSpec((CONV_WIDTH, LANES), lambda i: (0, i)), seq(0), ANY],
        out_specs=[ANY, pl.BlockSpec((CONV_PAD, LANES), lambda i: (0, i)), pl.BlockSpec((1, LANES), lambda i: (0, i))],
        out_shape=[jax.ShapeDtypeStruct(dproj.shape, BF16),
                   jax.ShapeDtypeStruct((CONV_PAD, C), F32), jax.ShapeDtypeStruct((1, C), F32)],
        input_output_aliases={4: 0},
        scratch_shapes=[pltpu.VMEM((S + CONV_PAD, LANES), F32), pltpu.VMEM((TIME_BLOCK + CONV_PAD, LANES), F32),
                        pltpu.VMEM((CONV_PAD, LANES), F32),
                        pltpu.VMEM((2, 2, S, LANES), BF16), pltpu.SemaphoreType.DMA((2, 2))],
        compiler_params=_params(("arbitrary",)),
    )(proj, proj, w, dc1, dproj)


def _ffn_fwd_call(u, w, b):
    S, C2 = u.shape
    C = C2 // 2
    nt = C // LANES

    def body(ug_ref, uv_ref, wg_ref, wv_ref, bg_ref, bv_ref, f_ref, hg_ref, hv_ref):
        _fill_head(hg_ref, ug_ref, FFN_PAD)
        _fill_head(hv_ref, uv_ref, FFN_PAD)

        def block(t0, xg_ref, xv_ref, x_t0, pad):
            cg = _conv_block(xg_ref, x_t0, wg_ref, FFN_CONV_WIDTH, pad) + bg_ref[...]
            cv = _conv_block(xv_ref, x_t0, wv_ref, FFN_CONV_WIDTH, pad) + bv_ref[...]
            f_ref[pl.ds(t0, TIME_BLOCK), :] = (_gelu(cg) * cv).astype(BF16)

        block(0, hg_ref, hv_ref, 0, FFN_PAD)
        _time_loop(S, lambda t0: block(t0, ug_ref, uv_ref, t0, 0), skip_first=1)

    seq = lambda off: pl.BlockSpec((S, LANES), lambda i: (0, off + i))
    wsp = lambda off: pl.BlockSpec((FFN_CONV_WIDTH, LANES), lambda i: (0, off + i))
    bsp = lambda off: pl.BlockSpec((1, LANES), lambda i: (0, off + i))
    return pl.pallas_call(
        body, name="ffn_conv_geglu", grid=(nt,),
        in_specs=[seq(0), seq(nt), wsp(0), wsp(nt), bsp(0), bsp(nt)],
        out_specs=seq(0), out_shape=jax.ShapeDtypeStruct((S, C), BF16),
        scratch_shapes=[pltpu.VMEM((FFN_PAD + TIME_BLOCK, LANES), F32)] * 2,
        compiler_params=_params(("parallel",)),
    )(u, u, w, w, b, b)


def _ffn_bwd_call(u, w, b, df):
    S, C2 = u.shape
    C = C2 // 2
    nt = C // LANES

    def body(ug_ref, uv_ref, wg_ref, wv_ref, bg_ref, bv_ref, df_ref,
             du_ref, dwg_ref, dwv_ref, dbg_ref, dbv_ref,
             hg_ref, hv_ref, dg_ref, dv_ref, dwg_acc, dwv_acc, dbg_acc, dbv_acc):
        zeros = jnp.zeros((FFN_PAD, LANES), F32)
        _fill_head(hg_ref, ug_ref, FFN_PAD)
        _fill_head(hv_ref, uv_ref, FFN_PAD)
        dg_ref[S:, :] = zeros
        dv_ref[S:, :] = zeros
        dwg_acc[...] = jnp.zeros_like(dwg_acc)
        dwv_acc[...] = jnp.zeros_like(dwv_acc)
        dbg_acc[...] = jnp.zeros_like(dbg_acc)
        dbv_acc[...] = jnp.zeros_like(dbv_acc)

        def first(t0, xg_ref, xv_ref, x_t0, pad):
            rows = pl.ds(t0, TIME_BLOCK)
            cg = _conv_block(xg_ref, x_t0, wg_ref, FFN_CONV_WIDTH, pad) + bg_ref[...]
            cv = _conv_block(xv_ref, x_t0, wv_ref, FFN_CONV_WIDTH, pad) + bv_ref[...]
            dfb = df_ref[rows, :]
            gelu, gelu_grad = _gelu_and_grad(cg)
            dcg = dfb * cv * gelu_grad
            dcv = dfb * gelu
            dg_ref[rows, :] = dcg
            dv_ref[rows, :] = dcv
            _conv_weight_grad(xg_ref, x_t0, dcg, dwg_acc, FFN_CONV_WIDTH, pad)
            _conv_weight_grad(xv_ref, x_t0, dcv, dwv_acc, FFN_CONV_WIDTH, pad)
            dbg_acc[...] += jnp.sum(dcg, axis=0, keepdims=True)
            dbv_acc[...] += jnp.sum(dcv, axis=0, keepdims=True)

        def second(t0):
            rows = pl.ds(t0, TIME_BLOCK)
            du_ref[0, rows, :] = _conv_transpose_block(dg_ref, t0, wg_ref, FFN_CONV_WIDTH).astype(BF16)
            du_ref[1, rows, :] = _conv_transpose_block(dv_ref, t0, wv_ref, FFN_CONV_WIDTH).astype(BF16)

        first(0, hg_ref, hv_ref, 0, FFN_PAD)
        _time_loop(S, lambda t0: first(t0, ug_ref, uv_ref, t0, 0), skip_first=1)
        _time_loop(S, second)
        dwg_ref[...] = dwg_acc[...]
        dwv_ref[...] = dwv_acc[...]
        dbg_ref[...] = dbg_acc[...]
        dbv_ref[...] = dbv_acc[...]

    seq = lambda off: pl.BlockSpec((S, LANES), lambda i: (0, off + i))
    wsp = lambda off: pl.BlockSpec((FFN_CONV_WIDTH, LANES), lambda i: (0, off + i))
    bsp = lambda off: pl.BlockSpec((1, LANES), lambda i: (0, off + i))
    return pl.pallas_call(
        body, name="ffn_conv_geglu_bwd", grid=(nt,),
        in_specs=[seq(0), seq(nt), wsp(0), wsp(nt), bsp(0), bsp(nt), seq(0)],
        out_specs=[pl.BlockSpec((2, S, LANES), lambda i: (0, 0, i)),
                   pl.BlockSpec((SUBLANES, LANES), lambda i: (0, i)), pl.BlockSpec((SUBLANES, LANES), lambda i: (0, i)),
                   bsp(0), bsp(0)],
        out_shape=[jax.ShapeDtypeStruct((2, S, C), BF16)] + [jax.ShapeDtypeStruct((SUBLANES, C), F32)] * 2
        + [jax.ShapeDtypeStruct((1, C), F32)] * 2,
        scratch_shapes=[pltpu.VMEM((FFN_PAD + TIME_BLOCK, LANES), F32)] * 2 + [pltpu.VMEM((S + FFN_PAD, LANES), F32)] * 2
        + [pltpu.VMEM((SUBLANES, LANES), F32)] * 2
        + [pltpu.VMEM((1, LANES), F32)] * 2,
        compiler_params=_params(("parallel",)),
    )(u, u, w, w, b, b, df)


def _adamw(w_ref, g_ref, m_ref, v_ref, d_ref, mo_ref, vo_ref):
    gv = g_ref[...]
    mn = ADAM_B1 * m_ref[...] + (1.0 - ADAM_B1) * gv
    vn = ADAM_B2 * v_ref[...] + (1.0 - ADAM_B2) * (gv * gv)
    mo_ref[...] = mn
    vo_ref[...] = vn
    m_hat = mn * (1.0 / (1.0 - ADAM_B1 ** ADAM_STEP))
    v_hat = vn * (1.0 / (1.0 - ADAM_B2 ** ADAM_STEP))
    d_ref[...] = -ADAM_LR * (m_hat / (jnp.sqrt(v_hat) + ADAM_EPS) + ADAM_WD * w_ref[...])


def _adamw_call(w, g, m, v, name):
    R, C = w.shape
    tr = _row_tile(R, C)

    def body(w_ref, g_ref, m_ref, v_ref, go_ref, d_ref, mo_ref, vo_ref):
        go_ref[...] = g_ref[...]
        _adamw(w_ref, g_ref, m_ref, v_ref, d_ref, mo_ref, vo_ref)

    spec = pl.BlockSpec((tr, C), lambda i: (i, 0))
    return pl.pallas_call(
        body, name=name, grid=(R // tr,),
        in_specs=[spec] * 4, out_specs=[spec] * 4,
        out_shape=[jax.ShapeDtypeStruct((R, C), F32)] * 4,
        compiler_params=_params(("parallel",)),
    )(w, g, m, v)


def _adamw_small_call(ws, gs, ms, vs):
    n = len(ws)

    def body(*refs):
        w_refs, g_refs, m_refs, v_refs, d_refs, mo_refs, vo_refs = (refs[i * n:(i + 1) * n] for i in range(7))
        for i in range(n):
            _adamw(w_refs[i], g_refs[i], m_refs[i], v_refs[i], d_refs[i], mo_refs[i], vo_refs[i])

    whole = pl.BlockSpec(memory_space=pltpu.VMEM)
    outs = pl.pallas_call(
        body, name="adamw_small",
        in_specs=[whole] * (4 * n), out_specs=[whole] * (3 * n),
        out_shape=[jax.ShapeDtypeStruct(w.shape, F32) for w in ws] * 3,
    )(*ws, *gs, *ms, *vs)
    return outs[:n], outs[n:2 * n], outs[2 * n:]


def _position():
    return lax.axis_index("x"), lax.axis_index("y"), lax.axis_index("c")


def _chip_peers(x, y):
    return [(x, 1 - y), (1 - x, y), (1 - x, 1 - y)]


def _half_rows(ref, core, rows):
    h = rows // 2
    start = pl.multiple_of(core * h, PACKED_ROWS)
    return ref.at[pl.ds(start, h), :] if len(ref.shape) == 2 else ref.at[:, pl.ds(start, h), :]


def _shard_half(ref, shard, core, rows):
    h = rows // 2
    return ref.at[shard, pl.ds(pl.multiple_of(core * h, PACKED_ROWS), h), :]


ANY = pl.BlockSpec(memory_space=pl.ANY)


def _first_hop_copies(srcs, lands):
    x, y, c = _position()
    chip = 2 * x + y
    targets = [(px, py, c) for px, py in _chip_peers(x, y)] + [(x, y, 1 - c)]
    rows = srcs[0].shape[0]
    out = []
    for i, (s, l) in enumerate(zip(srcs, lands)):
        for k, dev in enumerate(targets):
            if i == 0 and k < 3:
                out.append((_half_rows(s, c, rows), _shard_half(l, chip, c, rows), dev, k))
            else:
                out.append((s, l.at[chip], dev, len(targets) * i + k))
    return out


def _second_hop_copies(srcs, lands):
    x, y, c = _position()
    rows = lands[0].shape[1]
    out = []
    for k, (px, py) in enumerate(_chip_peers(x, y)):
        half = _shard_half(lands[0], 2 * px + py, c, rows)
        out.append((half, half, (x, y, 1 - c), k))
    return out


HBM_SPEC = pl.BlockSpec(memory_space=pltpu.HBM)
SEM_SPEC = pl.BlockSpec(memory_space=pltpu.SEMAPHORE)
DATAFLOW = pltpu.SideEffectType.DATAFLOW_SIDE_EFFECTING


def _in_hbm(a):
    return pltpu.with_memory_space_constraint(a, pltpu.HBM)


def _split_start(name, groups, after, carry=None):
    spans, arrays = [], []
    for srcs, lands, _, _ in groups:
        spans.append((len(arrays), len(srcs), len(lands)))
        arrays += list(srcs) + list(lands)
    if carry is not None:
        arrays.append(carry)
    na, ng = len(arrays), len(groups)

    def body(*refs):
        sems, token = refs[na + 1:na + 1 + 2 * ng], refs[-1]
        for g, (_, _, _, copies) in enumerate(groups):
            off, ns, nl = spans[g]
            for src, dst, dev, idx in copies(refs[off:off + ns], refs[off + ns:off + ns + nl]):
                pltpu.make_async_remote_copy(src_ref=src, dst_ref=dst, send_sem=sems[2 * g].at[idx], recv_sem=sems[2 * g + 1].at[idx],
                                             device_id=dev, device_id_type=MESH).start()
        token[...] = jnp.zeros_like(token)

    outs = pl.pallas_call(
        body, name=name,
        in_specs=[HBM_SPEC] * na + [ANY],
        out_specs=[SEM_SPEC] * (2 * ng) + [HBM_SPEC] * na + [pl.BlockSpec(memory_space=pltpu.VMEM)],
        out_shape=[pltpu.SemaphoreType.DMA((n_sems,)) for _, _, n_sems, _ in groups for _ in range(2)]
        + [pltpu.HBM(a.shape, a.dtype) for a in arrays] + [jax.ShapeDtypeStruct((SUBLANES, LANES), F32)],
        input_output_aliases={i: 2 * ng + i for i in range(na)},
        compiler_params=pltpu.CompilerParams(has_side_effects=DATAFLOW),
    )(*[_in_hbm(a) for a in arrays], after)
    started = []
    for g, (off, ns, nl) in enumerate(spans):
        thru = outs[2 * ng + off:2 * ng + off + ns + nl]
        started.append(dict(send=outs[2 * g], recv=outs[2 * g + 1], srcs=list(thru[:ns]), lands=list(thru[ns:]),
                            tile=outs[-1], token=outs[-1][0, 0], carry=None if carry is None else outs[2 * ng + na - 1]))
    return started


def _split_wait(name, started, copies, after):
    n, m = len(started["srcs"]), len(started["lands"])
    after = list(after) if isinstance(after, (list, tuple)) else [after]

    def body(*refs):
        src_refs, land_refs = refs[:n], refs[n:n + m]
        send_sem, recv_sem = refs[n + m], refs[n + m + 1]
        for src, dst, dev, idx in copies(src_refs, land_refs):
            cp = pltpu.make_async_remote_copy(src_ref=src, dst_ref=dst, send_sem=send_sem.at[idx], recv_sem=recv_sem.at[idx],
                                              device_id=dev, device_id_type=MESH)
            cp.wait_send()
            cp.wait_recv()

    arrays = started["srcs"] + started["lands"]
    outs = pl.pallas_call(
        body, name=name,
        in_specs=[HBM_SPEC] * (n + m) + [SEM_SPEC, SEM_SPEC] + [ANY] * len(after),
        out_specs=[HBM_SPEC] * (n + m),
        out_shape=[pltpu.HBM(a.shape, a.dtype) for a in arrays],
        input_output_aliases={i: i for i in range(n + m)},
        compiler_params=pltpu.CompilerParams(has_side_effects=DATAFLOW),
    )(*arrays, started["send"], started["recv"], *after)
    return list(outs)


def _gather_copies(srcs, lands):
    x, y, c = _position()
    chip = 2 * x + y
    targets = [(px, py, c) for px, py in _chip_peers(x, y)] + [(x, y, 1 - c)]
    return [(s, l.at[chip], dev, len(targets) * i + k) for i, (s, l) in enumerate(zip(srcs, lands)) for k, dev in enumerate(targets)]


def _sibling_copies(srcs, lands):
    x, y, c = _position()
    return [(_half_rows(srcs[0], 1 - c, srcs[0].shape[1]), lands[0], (x, y, 1 - c), 0)]


def _sibling_whole_copies(srcs, lands):
    x, y, c = _position()
    return [(srcs[0], lands[0], (x, y, 1 - c), 0)]


def _exchange_copies(srcs, lands):
    x, y, c = _position()
    return [(srcs[0].at[2 * px + py], lands[0].at[k], (px, py, c), k) for k, (px, py) in enumerate(_chip_peers(x, y))]


def _pair_sum_call(grad, recv, chip_core, name):
    _, h, B = recv.shape
    tr = _row_tile(h, B)

    def body(cc_ref, g_ref, r_ref, o_ref, ob_ref):
        s = g_ref[...] + r_ref[...]
        ob_ref[...] = s.astype(BF16)

        @pl.when(pl.program_id(1) == cc_ref[0])
        def _():
            o_ref[...] = s

    g_spec = pl.BlockSpec((None, tr, B), lambda i, q, cc_ref: (q, cc_ref[1] * (h // tr) + i, 0))
    spec = pl.BlockSpec((None, tr, B), lambda i, q, cc_ref: (q, i, 0))
    own_spec = pl.BlockSpec((tr, B), lambda i, q, cc_ref: (i, 0))
    return pl.pallas_call(
        body, name=name,
        grid_spec=pltpu.PrefetchScalarGridSpec(num_scalar_prefetch=1, grid=(h // tr, N_CHIPS), in_specs=[g_spec, spec],
                                               out_specs=[own_spec, spec]),
        out_shape=[jax.ShapeDtypeStruct((h, B), F32), jax.ShapeDtypeStruct(recv.shape, BF16)],
        compiler_params=_params(("parallel", "arbitrary")),
    )(chip_core, grad, recv)


def _chip_sum_call(partial, recv, chip_core, name):
    _, h, B = recv.shape
    tr = _row_tile(h, B)

    def body(cc_ref, p_ref, r_ref, o_ref):
        o_ref[...] = ((p_ref[...] + r_ref[0].astype(F32)) + r_ref[1].astype(F32)) + r_ref[2].astype(F32)

    return pl.pallas_call(
        body, name=name,
        grid_spec=pltpu.PrefetchScalarGridSpec(
            num_scalar_prefetch=1, grid=(h // tr,),
            in_specs=[pl.BlockSpec((tr, B), lambda i, cc_ref: (i, 0)),
                      pl.BlockSpec((3, tr, B), lambda i, cc_ref: (0, i, 0))],
            out_specs=pl.BlockSpec((tr, B), lambda i, cc_ref: (cc_ref[1] * (h // tr) + i, 0))),
        out_shape=jax.ShapeDtypeStruct((2 * h, B), F32),
        compiler_params=_params(("parallel",)),
    )(chip_core, partial, recv)


def _assemble_copies(srcs, lands):
    x, y, c = _position()
    out = []
    for i, land in enumerate(lands):
        half = _half_rows(land, c, land.shape[0])
        out.append((half, half, (x, y, 1 - c), i))
    return out


N_DEVICES = 8


def _allsum_copies(srcs, lands):
    x, y, c = _position()
    me = 4 * x + 2 * y + c
    out = []
    for k in range(1, N_DEVICES):
        peer = (1 - x if k & 4 else x, 1 - y if k & 2 else y, 1 - c if k & 1 else c)
        out.append((srcs[0], lands[0].at[me], peer, k - 1))
    return out


def _ordered_sum_call(mine, landed, me_chip, shapes, sharded_cols):
    rows = mine.shape[0]
    outs = [(s[0], n) if n else s for s, n in zip(shapes, sharded_cols)]

    def body(mc_ref, x_ref, l_ref, *refs):
        acc_ref = refs[-1]
        acc = jnp.where(mc_ref[0] == 0, x_ref[...], l_ref[0])
        for d in range(1, N_DEVICES):
            acc = acc + jnp.where(mc_ref[0] == d, x_ref[...], l_ref[d])
        acc_ref[...] = acc
        first = 0
        for o_ref, (r, c), n in zip(refs[:-1], shapes, sharded_cols):
            per_row = c // LANES

            def unpack(chip, o_ref=o_ref, r=r, n=n, per_row=per_row, first=first):
                for i in range(r):
                    for j in range((n or per_row * LANES) // LANES):
                        src = first + i * per_row + chip * ((n or 0) // LANES) + j
                        o_ref[i:i + 1, j * LANES:(j + 1) * LANES] = acc_ref[src:src + 1, :]

            if n:
                for q in range(N_CHIPS):
                    pl.when(mc_ref[1] == q)(functools.partial(unpack, q))
            else:
                unpack(0)
            first += r * per_row

    results = pl.pallas_call(
        body, name="small_grad_sum",
        in_specs=[pl.BlockSpec(memory_space=pltpu.SMEM), pl.BlockSpec(memory_space=pltpu.VMEM), pl.BlockSpec(memory_space=pltpu.VMEM)],
        out_specs=[pl.BlockSpec(memory_space=pltpu.VMEM)] * len(outs),
        out_shape=[jax.ShapeDtypeStruct(s, F32) for s in outs],
        scratch_shapes=[pltpu.VMEM((rows, LANES), F32)],
    )(me_chip, mine, landed)
    return results


def _pack(arrays):
    flat = jnp.concatenate([a.reshape(-1).astype(F32) for a in arrays])
    rows = -(-flat.shape[0] // LANES)
    rows = -(-rows // SUBLANES) * SUBLANES
    flat = jnp.pad(flat, (0, rows * LANES - flat.shape[0]))
    return flat.reshape(rows, LANES)


def _local_step(xs, target, P, late_weights, on_grad):
    S, D = xs.shape
    qkv_width = 3 * N_HEADS * HEAD_DIM
    glu_col0, gate_col0 = qkv_width, qkv_width + 2 * D
    shard_major = lambda g: g.reshape(N_CHIPS, g.shape[0] // N_CHIPS, g.shape[1])

    h1 = _rms_fwd_call(xs, P["norm_mix_pre"])
    buckets = _bucket_tables()
    bias = _bias_table_call(P["rel_bias"] + 0.0 * h1[0, 0].astype(F32), buckets)
    P = dict(P, **late_weights("in", bias))
    proj = _matmul(h1, P["w_in"], "nn", "proj_in")
    parts = []
    for g in range(N_GROUPS):
        parts += _attn_fwd_call(proj, bias, g)
    a, a_bf, lse = _attn_merge_call(parts)
    P = dict(P, **late_weights("mix", a_bf))
    y_a = _matmul(a_bf, P["w_attn_out"], "nn", "attn_out")
    c1 = _conv_fwd_call(proj, glu_col0, P["conv_dw_w"], P["conv_dw_b"])
    cact = _ln_silu_call(c1, P["conv_ln_g"], P["conv_ln_b"])
    y_c = _matmul(cact, P["conv_pw_w"], "nn", "conv_pw")
    mixed = _mix_call(proj, gate_col0, P["b_gate"], y_a, y_c)
    out = _matmul(mixed, P["w_out"], "nn", "mix_out")
    x1, h2 = _res1_call(xs, out, P["norm_mix_post"], P["norm_ffn_pre"])
    P = dict(P, **late_weights("up", h2))
    u = _matmul(h2, P["w_up"], "nn", "ffn_up")
    f = _ffn_fwd_call(u, P["ffn_conv_w"], P["ffn_conv_b"])
    P = dict(P, **late_weights("down", f))
    yff = _matmul(f, P["w_down"], "nn", "ffn_down")
    loss_tile, dx2, dyff, dg_ffn_post = _loss_call(yff, x1, P["norm_ffn_post"], target)

    G = {}
    G["norm_ffn_post"] = dg_ffn_post
    on_grad("w_down", shard_major(_matmul(f, dyff, "tn", "ffn_down_dw")))
    df = _matmul(dyff, P["w_down"], "nt", "ffn_down_dx")
    du, dwg, dwv, dbg, dbv = _ffn_bwd_call(u, P["ffn_conv_w"], P["ffn_conv_b"], df)
    G["ffn_conv_w"] = jnp.concatenate([dwg[:FFN_CONV_WIDTH], dwv[:FFN_CONV_WIDTH]], axis=1)
    G["ffn_conv_b"] = jnp.concatenate([dbg, dbv], axis=1)
    du = on_grad("w_up", functools.partial(_grad_half_matmul, h2, "ffn_up_dw"), carry=du)
    dh2 = _matmul(du, P["w_up"], "nt", "ffn_up_dx")
    dx1, dout, G["norm_ffn_pre"], G["norm_mix_post"] = _mid_bwd_call(x1, P["norm_ffn_pre"], dh2, dx2, out, P["norm_mix_post"])
    on_grad("w_out", shard_major(_matmul(mixed, dout, "tn", "mix_out_dw")))
    dmixed = _matmul(dout, P["w_out"], "nt", "mix_out_dx")
    dya, dyc, dproj, dba, dbc = _mix_bwd_call(dmixed, proj, gate_col0, P["b_gate"], y_a, y_c)
    G["b_gate"] = jnp.concatenate([dba, dbc], axis=1)
    on_grad("w_attn_out", _matmul(a_bf, dya, "tn", "attn_out_dw", out_shards=True))
    dyc = on_grad("conv_pw_w", shard_major(_matmul(cact, dyc, "tn", "conv_pw_dw")), carry=dyc)
    da = _matmul(dya, P["w_attn_out"], "nt", "attn_out_dx")
    dcact = _matmul(dyc, P["conv_pw_w"], "nt", "conv_pw_dx")
    dc1, G["conv_ln_g"], G["conv_ln_b"] = _ln_silu_bwd_call(c1, P["conv_ln_g"], P["conv_ln_b"], dcact)
    dproj, dw_dw, G["conv_dw_b"] = _conv_bwd_call(proj, glu_col0, P["conv_dw_w"], dc1, dproj)
    G["conv_dw_w"] = dw_dw[:CONV_WIDTH]
    delta = _attn_delta_call(a, da)
    dbs = []
    for g in range(N_GROUPS):
        dproj, db = _attn_bwd_call(proj, bias, da, lse, delta, g, dproj)
        dbs.append(db)
    G["rel_bias"] = _bias_grad_call(jnp.concatenate(dbs, axis=0), buckets)
    dproj = on_grad("w_in", functools.partial(_grad_half_matmul, h1, "proj_in_dw"), carry=dproj)
    dh1 = _matmul(dproj, P["w_in"], "nt", "proj_in_dx")
    dh1 = on_grad(None, None, carry=dh1)
    grad_x, G["norm_mix_pre"] = _in_bwd_call(xs, P["norm_mix_pre"], dh1, dx1)
    return loss_tile, grad_x, G


def kernel(x, w_in, b_gate, rel_bias, w_attn_out, conv_dw_w, conv_dw_b, conv_ln_g, conv_ln_b, conv_pw_w, w_out, norm_mix_pre, norm_mix_post, norm_ffn_pre, norm_ffn_post, w_up, ffn_conv_w, ffn_conv_b, w_down, loss_target, m_w_in, m_b_gate, m_rel_bias, m_w_attn_out, m_conv_dw_w, m_conv_dw_b, m_conv_ln_g, m_conv_ln_b, m_conv_pw_w, m_w_out, m_norm_mix_pre, m_norm_mix_post, m_norm_ffn_pre, m_norm_ffn_post, m_w_up, m_ffn_conv_w, m_ffn_conv_b, m_w_down, v_w_in, v_b_gate, v_rel_bias, v_w_attn_out, v_conv_dw_w, v_conv_dw_b, v_conv_ln_g, v_conv_ln_b, v_conv_pw_w, v_w_out, v_norm_mix_pre, v_norm_mix_post, v_norm_ffn_pre, v_norm_ffn_post, v_w_up, v_ffn_conv_w, v_ffn_conv_b, v_w_down):
    weights = dict(w_in=w_in, b_gate=b_gate, rel_bias=rel_bias, w_attn_out=w_attn_out, conv_dw_w=conv_dw_w, conv_dw_b=conv_dw_b,
                   conv_ln_g=conv_ln_g, conv_ln_b=conv_ln_b, conv_pw_w=conv_pw_w, w_out=w_out, norm_mix_pre=norm_mix_pre,
                   norm_mix_post=norm_mix_post, norm_ffn_pre=norm_ffn_pre, norm_ffn_post=norm_ffn_post, w_up=w_up,
                   ffn_conv_w=ffn_conv_w, ffn_conv_b=ffn_conv_b, w_down=w_down)
    m_in = dict(w_in=m_w_in, b_gate=m_b_gate, rel_bias=m_rel_bias, w_attn_out=m_w_attn_out, conv_dw_w=m_conv_dw_w,
                conv_dw_b=m_conv_dw_b, conv_ln_g=m_conv_ln_g, conv_ln_b=m_conv_ln_b, conv_pw_w=m_conv_pw_w, w_out=m_w_out,
                norm_mix_pre=m_norm_mix_pre, norm_mix_post=m_norm_mix_post, norm_ffn_pre=m_norm_ffn_pre,
                norm_ffn_post=m_norm_ffn_post, w_up=m_w_up, ffn_conv_w=m_ffn_conv_w, ffn_conv_b=m_ffn_conv_b, w_down=m_w_down)
    v_in = dict(w_in=v_w_in, b_gate=v_b_gate, rel_bias=v_rel_bias, w_attn_out=v_w_attn_out, conv_dw_w=v_conv_dw_w,
                conv_dw_b=v_conv_dw_b, conv_ln_g=v_conv_ln_g, conv_ln_b=v_conv_ln_b, conv_pw_w=v_conv_pw_w, w_out=v_w_out,
                norm_mix_pre=v_norm_mix_pre, norm_mix_post=v_norm_mix_post, norm_ffn_pre=v_norm_ffn_pre,
                norm_ffn_post=v_norm_ffn_post, w_up=v_w_up, ffn_conv_w=v_ffn_conv_w, ffn_conv_b=v_ffn_conv_b, w_down=v_w_down)
    names = list(weights)
    xi, yi, ci = _position()
    chip = 2 * xi + yi
    core_arr = jnp.reshape(ci, (1,)).astype(jnp.int32)

    xs = x[0]
    target = loss_target[0]
    S, D = xs.shape

    big = ["w_in", "w_attn_out", "conv_pw_w", "w_out", "w_up", "w_down"]
    row_sharded = ("conv_pw_w", "w_out", "w_down")
    natural = lambda k, g: g.reshape(-1, g.shape[2]) if k in row_sharded else g
    first_srcs = [w_in[0].astype(BF16), conv_dw_w[0], ffn_conv_w[0]]
    first_lands = [lax.empty((N_CHIPS,) + s.shape, s.dtype) for s in first_srcs]
    (first_hop,) = _split_start("gather_in_start", [(first_srcs, first_lands, 4 * len(first_srcs), _first_hop_copies)], core_arr)
    launched = first_hop["token"]
    late_sets = dict(mix=["w_attn_out", "conv_pw_w", "w_out"], up=["w_up"], down=["w_down"])
    late_groups = []
    for keys in late_sets.values():
        srcs = [(weights[k][0] + launched).astype(BF16) for k in keys]
        late_groups.append((srcs, [lax.empty((N_CHIPS,) + s.shape, BF16) for s in srcs], 4 * len(keys), _gather_copies))
    started = {}

    def late_weights(tag, after):
        if tag == "in":
            casts = [s for srcs, _, _, _ in late_groups for s in srcs]
            w_in_halves, dw4, fc4 = _split_wait("gather_in_wait", first_hop, _first_hop_copies, [after] + casts)[len(first_srcs):]
            second_hop, *late = _split_start("gather_in_pass_start", [([], [w_in_halves], 3, _second_hop_copies)] + late_groups, dw4)
            started.update(zip(late_sets, late))
            (w_in_full,) = _split_wait("gather_in_pass_wait", second_hop, _second_hop_copies, second_hop["tile"])
            return dict(w_in=w_in_full, conv_dw_w=jnp.concatenate(list(dw4), axis=1), ffn_conv_w=jnp.concatenate(list(fc4), axis=1))
        landed = _split_wait(f"gather_{tag}_wait", started[tag], _gather_copies, after)[len(late_sets[tag]):]
        return {k: natural(k, g) for k, g in zip(late_sets[tag], landed)}

    chip_core = jnp.stack([chip, ci]).astype(jnp.int32)
    exchanging, pending, second_half = {}, {}, {}

    held = []

    def launch(tag, after, carry=None):
        keys, groups, partial = [], [], {}
        for k in list(exchanging):
            st, copies = exchanging.pop(k)
            gk, r1 = _split_wait(f"sibling_exchange_wait_{k}", st, copies, after)
            if k in second_half:
                partial[k], s16 = second_half.pop(k)(init=r1)
            else:
                partial[k], s16 = _pair_sum_call(gk, r1, chip_core, f"pair_sum_{k}")
            keys.append(k)
            groups.append(([s16], [lax.empty((3,) + s16.shape[1:], BF16)], 3, _exchange_copies))
        fresh = [(k, copies) for k, _, copies in held]
        for _, g3, copies in held:
            rows = g3.shape[1] // 2 if copies is _sibling_copies else g3.shape[1]
            groups.append(([g3], [lax.empty((N_CHIPS, rows, g3.shape[2]), F32)], 1, copies))
        held.clear()
        begun = _split_start(f"grad_exchange_start_{tag}", groups, core_arr, carry)
        for k, st in zip(keys, begun):
            pending[k] = (partial[k], st)
        for (k, copies), st in zip(fresh, begun[len(keys):]):
            exchanging[k] = (st, copies)
        return begun[0]["carry"]

    others = [k for k in big if k != "w_in"]
    assembling = {}

    def on_grad(k, g, carry=None):
        if k is None:
            carried = launch("last", carry[:SUBLANES, :LANES], carry)
            assembling["others"] = assemble_start(others, carried, "others", carried)
            return assembling["others"]["carry"]
        if callable(g):
            theirs = g(jnp.stack([chip, 1 - ci]).astype(jnp.int32), carry)
            held.append((k, theirs, _sibling_whole_copies))
            carried = launch(k, theirs[0, :SUBLANES, :LANES], carry)
            second_half[k] = functools.partial(g, chip_core, carried)
            return carried
        held.append((k, g, _sibling_copies))
        if k in ("w_down", "w_out", "w_attn_out"):
            return carry
        return launch(k, g[0, :SUBLANES, :LANES], carry)

    def assemble_start(keys, after, tag, carry=None):
        halves = []
        for k in keys:
            s32, st = pending[k]
            recv2 = _split_wait(f"chip_exchange_wait_{k}", st, _exchange_copies, after)[1]
            halves.append(_chip_sum_call(s32, recv2, chip_core, f"chip_sum_{k}"))
        (st,) = _split_start(f"grad_assemble_start_{tag}", [([], halves, len(halves), _assemble_copies)], core_arr, carry)
        return st

    def assemble_wait(keys, st, after, tag):
        return dict(zip(keys, _split_wait(f"grad_assemble_wait_{tag}", st, _assemble_copies, after)))

    P = dict(b_gate=b_gate, rel_bias=rel_bias, conv_dw_b=conv_dw_b, conv_ln_g=conv_ln_g, conv_ln_b=conv_ln_b,
             norm_mix_pre=norm_mix_pre + launched, norm_mix_post=norm_mix_post, norm_ffn_pre=norm_ffn_pre,
             norm_ffn_post=norm_ffn_post, ffn_conv_b=ffn_conv_b)
    loss_tile, grad_x, G = _local_step(xs, target, P, late_weights, on_grad)

    small = [k for k in names if k not in big]
    packed = _pack([loss_tile[:1]] + [G[k] for k in small])
    (allsum,) = _split_start("small_grad_allsum_start",
                             [([packed], [jnp.zeros((N_DEVICES,) + packed.shape, F32)], N_DEVICES - 1, _allsum_copies)], core_arr)

    reduced, grads, deltas, new_m, new_v = {}, {}, {}, {}, {}

    def update(keys):
        for k in keys:
            gk, d, mn, vn = _adamw_call(weights[k][0], reduced[k], m_in[k][0], v_in[k][0], f"adamw_{k}")
            grads[k], deltas[k], new_m[k], new_v[k] = gk[None], d[None], mn[None], vn[None]

    reduced.update(assemble_wait(others, assembling["others"], [allsum["tile"], grad_x], "others"))
    update(others)
    assembling["w_in"] = assemble_start(["w_in"], [deltas[k] for k in others], "w_in")

    me_chip = jnp.stack([4 * xi + 2 * yi + ci, chip]).astype(jnp.int32)
    mine, landed = _split_wait("small_grad_allsum_wait", allsum, _allsum_copies, assembling["w_in"]["tile"])
    piece_shapes = [(1, LANES)] + [(G[k].size // LANES, LANES) if k == "rel_bias" else G[k].shape for k in small]
    piece_cols = [0] + [weights[k].shape[2] if k in ("conv_dw_w", "ffn_conv_w") else 0 for k in small]
    loss_row, *summed = _ordered_sum_call(mine, landed, me_chip, piece_shapes, piece_cols)
    loss = loss_row[0, 0]
    for k, gsum in zip(small, summed):
        grads[k] = gsum.reshape(weights[k].shape)
    ds, mns, vns = _adamw_small_call([weights[k] for k in small], [grads[k] for k in small],
                                     [m_in[k] for k in small], [v_in[k] for k in small])
    deltas.update(zip(small, ds))
    new_m.update(zip(small, mns))
    new_v.update(zip(small, vns))
    reduced.update(assemble_wait(["w_in"], assembling["w_in"], list(ds), "w_in"))
    update(["w_in"])

    return (loss, grad_x[None], *[grads[k] for k in names], *[deltas[k] for k in names],
            *[new_m[k] for k in names], *[new_v[k] for k in names])
```

```python
import functools
import math

import jax
import jax.numpy as jnp
import numpy as np
from jax import lax
from jax.experimental import pallas as pl
from jax.experimental.pallas import tpu as pltpu

F32 = jnp.float32
BF16 = jnp.bfloat16
MESH = pl.DeviceIdType.MESH

HEAD_DIM = 128
HEADS_PER_GROUP = 4
DILATED_PATTERNS = ((128, 1), (512, 4), (2048, 16))
N_GROUPS = 3
N_HEADS = N_GROUPS * HEADS_PER_GROUP
SPAN = 128
GROUP_WIDTH = HEADS_PER_GROUP * HEAD_DIM
CONV_WIDTH = 31
FFN_CONV_WIDTH = 3
N_BUCKETS = 32
MAX_DISTANCE = 2048
RMS_EPS = 1e-6
LN_EPS = 1e-5
NEG_INF = -1e30
ADAM_LR = 0.001
ADAM_B1 = 0.9
ADAM_B2 = 0.999
ADAM_EPS = 1e-08
ADAM_WD = 0.01
ADAM_STEP = 10

LANES = 128
SUBLANES = 8
PACKED_ROWS = 16
ROW_TILE = 512
GATE_ROWS, GATE_COLS = 512, 512
TIME_BLOCK = 128
CONV_PAD = 32
FFN_PAD = 8
VMEM_LIMIT = 56 << 20


def _params(sem=None, vmem=None):
    kw = {}
    if sem is not None:
        kw["dimension_semantics"] = sem
    if vmem is not None:
        kw["vmem_limit_bytes"] = vmem
    return pltpu.CompilerParams(**kw)


def _pick(n, cands):
    for c in cands:
        if n % c == 0:
            return c
    return n


ELEMENTWISE_TILE_BYTES = 3 << 19


def _row_tile(rows, cols):
    for align in (16, SUBLANES):
        fits = [t for t in range(align, rows + 1, align) if rows % t == 0 and t * cols * 4 <= ELEMENTWISE_TILE_BYTES]
        if fits:
            return max(fits)
    return SUBLANES


N_CHIPS = 4
M_TILES = (1024, 1408, 512, 256, 128)
N_TILES = (1024, 512, 1408, 256, 128)
K_TILES = (2176, 2048, 1408, 1024, 512, 256, 128)
WHOLE_B_M_TILE = 512


def _matmul(a, b, mode, name, out_shards=False, tm=None):
    assert a.dtype == BF16 and b.dtype == BF16, (name, a.dtype, b.dtype)
    b3 = b.ndim == 3
    tn = tk = None
    halves = None
    if mode == "nn":
        M, K = a.shape
        N = b.shape[-1] * (N_CHIPS if b3 else 1)
        tn = b.shape[-1] if b3 else None
    elif mode == "nt":
        if a.ndim == 3:
            halves = a.shape[2]
        M, K = a.shape[-2], a.shape[-1] * (a.shape[0] if a.ndim == 3 else 1)
        N = b.shape[-2]
        tk = b.shape[-1] if b3 else None
    else:
        if b3:
            halves = b.shape[2]
        K, M = a.shape
        N = b.shape[-1] * (b.shape[0] if b3 else 1)
        tn = N // N_CHIPS if out_shards else None
    tn = tn or _pick(N, N_TILES)
    tk = tk or _pick(K, K_TILES)
    nk = K // tk
    if tm is None and N == tn and nk == 1 and not halves and M % WHOLE_B_M_TILE == 0:
        tm = WHOLE_B_M_TILE
    tm = tm or _pick(M, M_TILES)
    dn = {"nn": (((1,), (0,)), ((), ())), "nt": (((1,), (1,)), ((), ())), "tn": (((0,), (0,)), ((), ()))}[mode]

    def body(a_ref, b_ref, o_ref):
        if nk == 1:
            o_ref[...] = lax.dot_general(a_ref[...], b_ref[...], dn, preferred_element_type=F32)
        else:
            @pl.when(pl.program_id(2) == 0)
            def _():
                o_ref[...] = jnp.zeros_like(o_ref)

            o_ref[...] += lax.dot_general(a_ref[...], b_ref[...], dn, preferred_element_type=F32)

    if mode == "tn":
        a_spec = pl.BlockSpec((tk, tm), lambda i, j, k: (k, i))
    elif halves:
        per = halves // tk
        a_spec = pl.BlockSpec((None, tm, tk), lambda i, j, k: (k // per, i, k % per))
    else:
        a_spec = pl.BlockSpec((tm, tk), lambda i, j, k: (i, k))
    if mode == "nn":
        b_spec = pl.BlockSpec((None, tk, tn), lambda i, j, k: (j, k, 0)) if b3 else pl.BlockSpec((tk, tn), lambda i, j, k: (k, j))
    elif mode == "nt":
        b_spec = pl.BlockSpec((None, tn, tk), lambda i, j, k: (k, j, 0)) if b3 else pl.BlockSpec((tn, tk), lambda i, j, k: (j, k))
    elif halves:
        per = halves // tn
        b_spec = pl.BlockSpec((None, tk, tn), lambda i, j, k: (j // per, k, j % per))
    else:
        b_spec = pl.BlockSpec((tk, tn), lambda i, j, k: (k, j))
    if out_shards:
        out_spec = pl.BlockSpec((None, tm, tn), lambda i, j, k: (j, i, 0))
        out_shape = jax.ShapeDtypeStruct((N_CHIPS, M, tn), F32)
    else:
        out_spec = pl.BlockSpec((tm, tn), lambda i, j, k: (i, j))
        out_shape = jax.ShapeDtypeStruct((M, N), F32)
    return pl.pallas_call(
        body, name=name, grid=(M // tm, N // tn, nk),
        in_specs=[a_spec, b_spec], out_specs=out_spec, out_shape=out_shape,
        compiler_params=_params(("parallel", "parallel", "arbitrary"), VMEM_LIMIT),
    )(a, b)


def _grad_half_matmul(a, name, chip_half, b, init=None):
    K, M = a.shape
    parts = b.ndim == 3
    N = b.shape[-1] * (b.shape[0] if parts else 1)
    h, tn = M // 2, N // N_CHIPS
    summed = init is not None
    dn = (((0,), (0,)), ((), ()))

    def body(ch_ref, a_ref, b_ref, *rest):
        product = lax.dot_general(a_ref[...], b_ref[...], dn, preferred_element_type=F32)
        if not summed:
            rest[0][...] = product
            return
        init_ref, own_ref, sum16_ref = rest
        total = product + init_ref[...]
        sum16_ref[...] = total.astype(BF16)

        @pl.when(pl.program_id(0) == ch_ref[0])
        def _():
            own_ref[...] = total

    if parts:
        per = b.shape[2] // tn
        b_spec = pl.BlockSpec((None, K, tn), lambda j, ch_ref: (j // per, 0, j % per))
    else:
        b_spec = pl.BlockSpec((K, tn), lambda j, ch_ref: (0, j))
    shard_spec = pl.BlockSpec((None, h, tn), lambda j, ch_ref: (j, 0, 0))
    own_spec = pl.BlockSpec((h, tn), lambda j, ch_ref: (0, 0))
    shape = (N_CHIPS, h, tn)
    return pl.pallas_call(
        body, name=name + ("_mine" if summed else "_theirs"),
        grid_spec=pltpu.PrefetchScalarGridSpec(
            num_scalar_prefetch=1, grid=(N_CHIPS,),
            in_specs=[pl.BlockSpec((K, h), lambda j, ch_ref: (0, ch_ref[1])), b_spec] + [shard_spec] * summed,
            out_specs=[own_spec, shard_spec] if summed else shard_spec),
        out_shape=[jax.ShapeDtypeStruct((h, tn), F32), jax.ShapeDtypeStruct(shape, BF16)] if summed
        else jax.ShapeDtypeStruct(shape, F32),
        compiler_params=_params(("arbitrary",), VMEM_LIMIT),
    )(chip_half, a, b, *([init] if summed else []))


def _rms(x, g):
    r = lax.rsqrt(jnp.mean(x * x, axis=-1, keepdims=True) + RMS_EPS)
    return x * r * g


def _rms_bwd(x, g, dy):
    r = lax.rsqrt(jnp.mean(x * x, axis=-1, keepdims=True) + RMS_EPS)
    n = x * r
    dn = dy * g
    dx = r * (dn - n * jnp.mean(dn * n, axis=-1, keepdims=True))
    return dx, jnp.sum(dy * n, axis=0, keepdims=True)


def _sigmoid(x):
    return 1.0 / (1.0 + jnp.exp(-x))


_GELU_C = math.sqrt(2.0 / math.pi)


def _gelu(x):
    return 0.5 * x * (1.0 + jnp.tanh(_GELU_C * (x + 0.044715 * x * x * x)))


def _gelu_and_grad(x):
    x2 = x * x
    t = jnp.tanh(_GELU_C * x * (1.0 + 0.044715 * x2))
    half = 0.5 * (1.0 + t)
    return x * half, half + (0.5 * _GELU_C) * x * (1.0 - t * t) * (1.0 + (3.0 * 0.044715) * x2)


def _row_spec(width, col_block=0):
    return pl.BlockSpec((ROW_TILE, width), lambda i: (i, col_block))


def _vec_spec(width, col_block=0):
    return pl.BlockSpec((1, width), lambda i: (0, col_block))


def _accumulate(ref, part):
    @pl.when(pl.program_id(0) == 0)
    def _():
        ref[...] = part

    @pl.when(pl.program_id(0) > 0)
    def _():
        ref[...] += part


def _rms_fwd_call(x, g):
    S, D = x.shape

    def body(x_ref, g_ref, h_ref):
        h_ref[...] = _rms(x_ref[...], g_ref[...]).astype(BF16)

    return pl.pallas_call(
        body, name="rms_mix_pre", grid=(S // ROW_TILE,),
        in_specs=[_row_spec(D), _vec_spec(D)], out_specs=_row_spec(D),
        out_shape=jax.ShapeDtypeStruct((S, D), BF16),
        compiler_params=_params(("parallel",)),
    )(x, g)


def _ln_silu_call(c1, g, b):
    S, C = c1.shape

    def body(c_ref, g_ref, b_ref, o_ref):
        xv = c_ref[...]
        mu = jnp.mean(xv, axis=-1, keepdims=True)
        xc = xv - mu
        var = jnp.mean(xc * xc, axis=-1, keepdims=True)
        z = xc * lax.rsqrt(var + LN_EPS) * g_ref[...] + b_ref[...]
        o_ref[...] = (z * _sigmoid(z)).astype(BF16)

    return pl.pallas_call(
        body, name="conv_ln_silu", grid=(S // ROW_TILE,),
        in_specs=[_row_spec(C), _vec_spec(C), _vec_spec(C)], out_specs=_row_spec(C),
        out_shape=jax.ShapeDtypeStruct((S, C), BF16),
        compiler_params=_params(("parallel",)),
    )(c1, g, b)


def _ln_silu_bwd_call(c1, g, b, dc):
    S, C = c1.shape

    def body(c_ref, g_ref, b_ref, dc_ref, dx_ref, dg_ref, db_ref):
        xv = c_ref[...]
        mu = jnp.mean(xv, axis=-1, keepdims=True)
        xc = xv - mu
        rs = lax.rsqrt(jnp.mean(xc * xc, axis=-1, keepdims=True) + LN_EPS)
        xh = xc * rs
        z = xh * g_ref[...] + b_ref[...]
        sg = _sigmoid(z)
        dz = dc_ref[...] * (sg * (1.0 + z * (1.0 - sg)))
        dxh = dz * g_ref[...]
        dx_ref[...] = rs * (dxh - jnp.mean(dxh, axis=-1, keepdims=True) - xh * jnp.mean(dxh * xh, axis=-1, keepdims=True))
        _accumulate(dg_ref, jnp.sum(dz * xh, axis=0, keepdims=True))
        _accumulate(db_ref, jnp.sum(dz, axis=0, keepdims=True))

    return pl.pallas_call(
        body, name="conv_ln_silu_bwd", grid=(S // ROW_TILE,),
        in_specs=[_row_spec(C), _vec_spec(C), _vec_spec(C), _row_spec(C)],
        out_specs=[_row_spec(C), _vec_spec(C), _vec_spec(C)],
        out_shape=[jax.ShapeDtypeStruct((S, C), F32), jax.ShapeDtypeStruct((1, C), F32), jax.ShapeDtypeStruct((1, C), F32)],
        compiler_params=_params(("arbitrary",)),
    )(c1, g, b, dc)


def _mix_call(proj, gate_col0, b_gate, y_a, y_c):
    S, D = y_a.shape
    w = GATE_COLS
    nc = D // w
    ga0, gc0 = gate_col0 // w, (gate_col0 + D) // w

    def body(ga_ref, gc_ref, ba_ref, bc_ref, ya_ref, yc_ref, o_ref):
        o_ref[...] = (_sigmoid(ga_ref[...] + ba_ref[...]) * ya_ref[...]
                      + _sigmoid(gc_ref[...] + bc_ref[...]) * yc_ref[...]).astype(BF16)

    tile = lambda off: pl.BlockSpec((GATE_ROWS, w), lambda i, j: (i, off + j))
    vec = lambda off: pl.BlockSpec((1, w), lambda i, j: (0, off + j))
    return pl.pallas_call(
        body, name="gate_mix", grid=(S // GATE_ROWS, nc),
        in_specs=[tile(ga0), tile(gc0), vec(0), vec(nc), tile(0), tile(0)],
        out_specs=tile(0), out_shape=jax.ShapeDtypeStruct((S, D), BF16),
        compiler_params=_params(("parallel", "parallel")),
    )(proj, proj, b_gate, b_gate, y_a, y_c)


def _window_stores(stage_ref, slot, dst_ref, rows, cols, sems):
    width = stage_ref.shape[-1]
    return [pltpu.make_async_copy(stage_ref.at[slot, p], dst_ref.at[rows, pl.ds(pl.multiple_of(c, LANES), width)], sems.at[slot, p])
            for p, c in enumerate(cols)]


def _staged_window_stores(stage_ref, dst_ref, sems, step, n_steps, rows, cols, fill):
    slot = step % 2
    copies = lambda s: _window_stores(stage_ref, s, dst_ref, rows, cols, sems)

    @pl.when(step >= 2)
    def _():
        for cp in copies(slot):
            cp.wait()

    fill(slot)
    for cp in copies(slot):
        cp.start()

    @pl.when(step == n_steps - 1)
    def _():
        for s in ([slot, 1 - slot] if n_steps > 1 else [slot]):
            for cp in copies(s):
                cp.wait()


def _mix_bwd_call(dmixed, proj, gate_col0, b_gate, y_a, y_c):
    S, D = y_a.shape
    w = GATE_COLS
    nc = D // w
    nr = S // GATE_ROWS
    ga0, gc0 = gate_col0 // w, (gate_col0 + D) // w

    def body(dm_ref, ga_ref, gc_ref, ba_ref, bc_ref, ya_ref, yc_ref, dya_ref, dyc_ref, dproj_ref, dba_ref, dbc_ref,
             stage_ref, sems):
        j, i = pl.program_id(0), pl.program_id(1)
        dm = dm_ref[...]
        sa = _sigmoid(ga_ref[...] + ba_ref[...])
        sc = _sigmoid(gc_ref[...] + bc_ref[...])
        dya_ref[...] = (dm * sa).astype(BF16)
        dyc_ref[...] = (dm * sc).astype(BF16)
        dga = dm * ya_ref[...] * sa * (1.0 - sa)
        dgc = dm * yc_ref[...] * sc * (1.0 - sc)

        def fill(slot):
            stage_ref[slot, 0] = dga.astype(BF16)
            stage_ref[slot, 1] = dgc.astype(BF16)

        rows = pl.ds(pl.multiple_of(i * GATE_ROWS, GATE_ROWS), GATE_ROWS)
        _staged_window_stores(stage_ref, dproj_ref, sems, j * nr + i, nc * nr, rows,
                              [gate_col0 + j * w, gate_col0 + D + j * w], fill)
        pa = jnp.sum(dga, axis=0, keepdims=True)
        pc = jnp.sum(dgc, axis=0, keepdims=True)

        @pl.when(i == 0)
        def _():
            dba_ref[...] = pa
            dbc_ref[...] = pc

        @pl.when(i > 0)
        def _():
            dba_ref[...] += pa
            dbc_ref[...] += pc

    tile = lambda off: pl.BlockSpec((GATE_ROWS, w), lambda j, i: (i, off + j))
    vec = lambda off: pl.BlockSpec((1, w), lambda j, i: (0, off + j))
    return pl.pallas_call(
        body, name="gate_mix_bwd", grid=(nc, nr),
        in_specs=[tile(0), tile(ga0), tile(gc0), vec(0), vec(nc), tile(0), tile(0)],
        out_specs=[tile(0), tile(0), ANY, vec(0), vec(0)],
        out_shape=[jax.ShapeDtypeStruct((S, D), BF16)] * 2 + [jax.ShapeDtypeStruct((S, proj.shape[1]), BF16)] + [
                   jax.ShapeDtypeStruct((1, D), F32), jax.ShapeDtypeStruct((1, D), F32)],
        scratch_shapes=[pltpu.VMEM((2, 2, GATE_ROWS, w), BF16), pltpu.SemaphoreType.DMA((2, 2))],
        compiler_params=_params(("arbitrary", "arbitrary")),
    )(dmixed, proj, proj, b_gate, b_gate, y_a, y_c)


def _res1_call(x, out, g_post, g_pre):
    S, D = x.shape

    def body(x_ref, o_ref, gp_ref, gq_ref, x1_ref, h2_ref):
        x1 = x_ref[...] + _rms(o_ref[...], gp_ref[...])
        x1_ref[...] = x1
        h2_ref[...] = _rms(x1, gq_ref[...]).astype(BF16)

    return pl.pallas_call(
        body, name="residual_mix", grid=(S // ROW_TILE,),
        in_specs=[_row_spec(D), _row_spec(D), _vec_spec(D), _vec_spec(D)],
        out_specs=[_row_spec(D), _row_spec(D)],
        out_shape=[jax.ShapeDtypeStruct((S, D), F32), jax.ShapeDtypeStruct((S, D), BF16)],
        compiler_params=_params(("parallel",)),
    )(x, out, g_post, g_pre)


def _loss_call(y, x1, g_post, target):
    S, D = y.shape

    def body(y_ref, x1_ref, g_ref, t_ref, loss_ref, dx_ref, dy_ref, dg_ref):
        yv, gv = y_ref[...], g_ref[...]
        err = x1_ref[...] + _rms(yv, gv) - t_ref[...]
        dx2 = err * (1.0 / D)
        dx_ref[...] = dx2
        dy, dg = _rms_bwd(yv, gv, dx2)
        dy_ref[...] = dy.astype(BF16)
        _accumulate(dg_ref, dg)
        part = 0.5 * jnp.sum(jnp.mean(err * err, axis=-1, keepdims=True), axis=0, keepdims=True)
        _accumulate(loss_ref, jnp.broadcast_to(part, (SUBLANES, LANES)))

    return pl.pallas_call(
        body, name="residual_ffn_loss", grid=(S // ROW_TILE,),
        in_specs=[_row_spec(D), _row_spec(D), _vec_spec(D), _row_spec(D)],
        out_specs=[pl.BlockSpec((SUBLANES, LANES), lambda i: (0, 0)), _row_spec(D), _row_spec(D), _vec_spec(D)],
        out_shape=[jax.ShapeDtypeStruct((SUBLANES, LANES), F32), jax.ShapeDtypeStruct((S, D), F32),
                   jax.ShapeDtypeStruct((S, D), BF16), jax.ShapeDtypeStruct((1, D), F32)],
        compiler_params=_params(("arbitrary",)),
    )(y, x1, g_post, target)


def _mid_bwd_call(x1, g_pre, dh2, dx2, out, g_post):
    S, D = x1.shape

    def body(x1_ref, gq_ref, dh_ref, dx2_ref, o_ref, gp_ref, dx1_ref, do_ref, dgq_ref, dgp_ref):
        d, dgq = _rms_bwd(x1_ref[...], gq_ref[...], dh_ref[...])
        dx1 = dx2_ref[...] + d
        dx1_ref[...] = dx1
        do, dgp = _rms_bwd(o_ref[...], gp_ref[...], dx1)
        do_ref[...] = do.astype(BF16)
        _accumulate(dgq_ref, dgq)
        _accumulate(dgp_ref, dgp)

    return pl.pallas_call(
        body, name="residual_mix_bwd", grid=(S // ROW_TILE,),
        in_specs=[_row_spec(D), _vec_spec(D), _row_spec(D), _row_spec(D), _row_spec(D), _vec_spec(D)],
        out_specs=[_row_spec(D), _row_spec(D), _vec_spec(D), _vec_spec(D)],
        out_shape=[jax.ShapeDtypeStruct((S, D), F32), jax.ShapeDtypeStruct((S, D), BF16)] + [jax.ShapeDtypeStruct((1, D), F32)] * 2,
        compiler_params=_params(("arbitrary",)),
    )(x1, g_pre, dh2, dx2, out, g_post)


def _in_bwd_call(x, g, dh1, dx1):
    S, D = x.shape

    def body(x_ref, g_ref, dh_ref, dx1_ref, gx_ref, dg_ref):
        d, dg = _rms_bwd(x_ref[...], g_ref[...], dh_ref[...])
        gx_ref[...] = dx1_ref[...] + d
        _accumulate(dg_ref, dg)

    return pl.pallas_call(
        body, name="rms_mix_pre_bwd", grid=(S // ROW_TILE,),
        in_specs=[_row_spec(D), _vec_spec(D), _row_spec(D), _row_spec(D)],
        out_specs=[_row_spec(D), _vec_spec(D)],
        out_shape=[jax.ShapeDtypeStruct((S, D), F32), jax.ShapeDtypeStruct((1, D), F32)],
        compiler_params=_params(("arbitrary",)),
    )(x, g, dh1, dx1)


def _bucket_table(dilation):
    qi = np.arange(SPAN)[:, None]
    ki = np.arange(2 * SPAN)[None, :]
    dist = np.maximum(qi + SPAN - ki, 0) * dilation
    max_exact = N_BUCKETS // 2
    d = np.maximum(dist, 1).astype(np.float64)
    large = max_exact + (np.log(d / max_exact) / math.log(MAX_DISTANCE / max_exact) * (N_BUCKETS - max_exact)).astype(np.int32)
    large = np.minimum(large, N_BUCKETS - 1)
    return np.where(dist < max_exact, dist, large).astype(np.int32)


def _bucket_tables():
    return jnp.asarray(np.stack([_bucket_table(r) for _, r in DILATED_PATTERNS]))


def _bias_table_call(rel_bias, buckets):
    def body(rb_ref, bk_ref, o_ref):
        for h in range(N_HEADS):
            bk = bk_ref[h // HEADS_PER_GROUP]

            def step(b, acc):
                return jnp.where(bk == b, rb_ref[b, h], acc)

            o_ref[h] = lax.fori_loop(0, N_BUCKETS, step, jnp.zeros((SPAN, 2 * SPAN), F32))

    return pl.pallas_call(
        body, name="rel_bias_table",
        in_specs=[pl.BlockSpec(memory_space=pltpu.SMEM), pl.BlockSpec(memory_space=pltpu.VMEM)],
        out_specs=pl.BlockSpec(memory_space=pltpu.VMEM),
        out_shape=jax.ShapeDtypeStruct((N_HEADS, SPAN, 2 * SPAN), F32),
    )(rel_bias, buckets)


def _bias_grad_call(dbias, buckets):
    def body(db_ref, bk_ref, o_ref, rows_ref):
        for h in range(N_HEADS):
            bk = bk_ref[h // HEADS_PER_GROUP]
            dv = db_ref[h]

            def step(b, carry):
                rows_ref[h, b] = jnp.sum(jnp.where(bk == b, dv, 0.0), axis=0, keepdims=True)
                return carry

            lax.fori_loop(0, N_BUCKETS, step, 0)
        o_ref[...] = jnp.sum(rows_ref[...], axis=-1, keepdims=True)

    out = pl.pallas_call(
        body, name="rel_bias_grad",
        in_specs=[pl.BlockSpec(memory_space=pltpu.VMEM), pl.BlockSpec(memory_space=pltpu.VMEM)],
        out_specs=pl.BlockSpec(memory_space=pltpu.VMEM),
        out_shape=jax.ShapeDtypeStruct((N_HEADS, N_BUCKETS, 1, 1), F32),
        scratch_shapes=[pltpu.VMEM((N_HEADS, N_BUCKETS, 1, 2 * SPAN), F32)],
    )(dbias, buckets)
    return out.reshape(N_HEADS, N_BUCKETS).T


def _dot_nt(a, b):
    return lax.dot_general(a, b, (((1,), (1,)), ((), ())), preferred_element_type=F32)


def _dot_nn(a, b):
    return lax.dot_general(a, b, (((1,), (0,)), ((), ())), preferred_element_type=F32)


def _dot_tn(a, b):
    return lax.dot_general(a, b, (((0,), (0,)), ((), ())), preferred_element_type=F32)


def _band_masks(n, nb):
    qi = lax.broadcasted_iota(jnp.int32, (SPAN, SPAN), 0)
    ki = lax.broadcasted_iota(jnp.int32, (SPAN, SPAN), 1)
    prev_ok = jnp.logical_and(ki >= qi, n > 0)
    cur_ok = ki <= qi
    next_ok = jnp.logical_and(ki >= qi, n < nb - 1)
    return prev_ok, cur_ok, next_ok


def _wide_band_mask(n):
    qi = lax.broadcasted_iota(jnp.int32, (SPAN, 2 * SPAN), 0)
    ki = lax.broadcasted_iota(jnp.int32, (SPAN, 2 * SPAN), 1)
    prev_ok = jnp.logical_and(jnp.logical_and(ki < SPAN, ki >= qi), n > 0)
    cur_ok = jnp.logical_and(ki >= SPAN, ki - SPAN <= qi)
    return jnp.logical_or(prev_ok, cur_ok)


def _attn_plan(S, group):
    r = DILATED_PATTERNS[group][1]
    hp, per = (HEADS_PER_GROUP, 1) if r == 1 else (2, 4)
    return r, S // (r * SPAN), hp, per


def _residue_rows(rho, r):
    return slice(None) if r == 1 else pl.ds(rho, SPAN, stride=r)


def _for_residues(r, per, fn):
    if r == per:
        for u in range(per):
            fn(u)
        return

    def step(i, carry):
        for u in range(per):
            fn(i * per + u)
        return carry

    lax.fori_loop(0, r // per, step, 0)


def _attn_fwd_call(proj, bias, group):
    S = proj.shape[0]
    r, nb, hp, per = _attn_plan(S, group)
    scale = HEAD_DIM ** -0.5
    kinds = ("q", "kp", "kc", "vp", "vc") if nb > 1 else ("q", "kc", "vc")

    per_kind = _refs_per_kind(r, hp)

    def body(*refs):
        ins = {kind: refs[i * per_kind:(i + 1) * per_kind] for i, kind in enumerate(kinds)}
        b_ref, o_ref, lse_ref = refs[len(kinds) * per_kind:]
        n = pl.program_id(1)
        prev_ok, cur_ok, _ = _band_masks(n, nb)

        band_ok = _wide_band_mask(n) if nb > 1 else cur_ok

        def residue(rho):
            rows = _residue_rows(rho, r)
            for j in range(hp):
                get = lambda kind: _head_rows(ins[kind], j, rows, r).astype(BF16)
                q = get("q")
                if nb > 1:
                    keys, vals, bias_j = jnp.concatenate([get("kp"), get("kc")], axis=0), jnp.concatenate([get("vp"), get("vc")], axis=0), b_ref[j]
                else:
                    keys, vals, bias_j = get("kc"), get("vc"), b_ref[j, :, SPAN:]
                s = jnp.where(band_ok, _dot_nt(q, keys) * scale + bias_j, NEG_INF)
                m = jnp.max(s, axis=-1, keepdims=True)
                p = jnp.exp(s - m)
                den = jnp.sum(p, axis=-1, keepdims=True)
                o_ref[j, rows, :] = _dot_nn(p.astype(BF16), vals) / den
                lse_ref[j, rows, :] = jnp.broadcast_to(m + jnp.log(den), (SPAN, HEAD_DIM))

        _for_residues(r, per, residue)

    in_specs = [_head_spec(r, nb, hp, kind, group, jj) for kind in kinds for jj in range(per_kind)]
    in_specs.append(pl.BlockSpec((hp, SPAN, 2 * SPAN), lambda j, n: (group * (HEADS_PER_GROUP // hp) + j, 0, 0)))
    out = pl.BlockSpec((hp, r * SPAN, HEAD_DIM), lambda j, n: (j, n, 0))
    return pl.pallas_call(
        body, name=f"attn_fwd_g{group}", grid=(HEADS_PER_GROUP // hp, nb),
        in_specs=in_specs, out_specs=[out] * 2,
        out_shape=[jax.ShapeDtypeStruct((HEADS_PER_GROUP, S, HEAD_DIM), F32)] * 2,
        compiler_params=_params(("parallel", "parallel"), VMEM_LIMIT),
    )(*([proj] * (len(in_specs) - 1)), bias)


_PROJ_PART = dict(q=0, qn=0, kp=1, kc=1, vp=2, vc=2)


def _refs_per_kind(r, hp):
    return 1 if r == 1 else hp


def _head_rows(refs, j, rows, r):
    return refs[0][:, j * HEAD_DIM:(j + 1) * HEAD_DIM] if r == 1 else refs[j][rows, :]


def _head_spec(r, nb, hp, kind, group, jj):
    if kind in _PROJ_PART:
        base = (_PROJ_PART[kind] * N_GROUPS + group) * HEADS_PER_GROUP
    else:
        base = 0
    if kind.endswith("p"):
        row = lambda n: jnp.maximum(n - 1, 0)
    elif kind.endswith("n"):
        row = lambda n: jnp.minimum(n + 1, nb - 1)
    else:
        row = lambda n: n
    if r == 1:
        return pl.BlockSpec((SPAN, hp * HEAD_DIM), lambda j, n: (row(n), base // hp + j))
    return pl.BlockSpec((r * SPAN, HEAD_DIM), lambda j, n: (row(n), base + j * hp + jj))


def _attn_merge_call(parts):
    S = parts[0].shape[1]

    def body(o1, s1, o2, s2, o3, s3, a_ref, ab_ref, lse_ref):
        for j in range(HEADS_PER_GROUP):
            sl = slice(j * HEAD_DIM, (j + 1) * HEAD_DIM)
            mx = jnp.maximum(jnp.maximum(s1[j], s2[j]), s3[j])
            w1 = jnp.exp(s1[j] - mx)
            w2 = jnp.exp(s2[j] - mx)
            w3 = jnp.exp(s3[j] - mx)
            den = w1 + w2 + w3
            a = (w1 * o1[j] + w2 * o2[j] + w3 * o3[j]) / den
            a_ref[:, sl] = a
            ab_ref[:, sl] = a.astype(BF16)
            lse_ref[:, sl] = mx + jnp.log(den)

    heads = pl.BlockSpec((HEADS_PER_GROUP, ROW_TILE, HEAD_DIM), lambda i: (0, i, 0))
    return pl.pallas_call(
        body, name="attn_merge", grid=(S // ROW_TILE,),
        in_specs=[heads] * 6, out_specs=[_row_spec(GROUP_WIDTH)] * 3,
        out_shape=[jax.ShapeDtypeStruct((S, GROUP_WIDTH), F32), jax.ShapeDtypeStruct((S, GROUP_WIDTH), BF16),
                   jax.ShapeDtypeStruct((S, GROUP_WIDTH), F32)],
        compiler_params=_params(("parallel",)),
    )(*parts)


def _attn_delta_call(a, da):
    S = a.shape[0]

    def body(a_ref, da_ref, d_ref):
        for j in range(HEADS_PER_GROUP):
            sl = slice(j * HEAD_DIM, (j + 1) * HEAD_DIM)
            d = jnp.sum(a_ref[:, sl] * da_ref[:, sl], axis=-1, keepdims=True)
            d_ref[:, sl] = jnp.broadcast_to(d, (ROW_TILE, HEAD_DIM))

    return pl.pallas_call(
        body, name="attn_delta", grid=(S // ROW_TILE,),
        in_specs=[_row_spec(GROUP_WIDTH)] * 2, out_specs=_row_spec(GROUP_WIDTH),
        out_shape=jax.ShapeDtypeStruct((S, GROUP_WIDTH), F32),
        compiler_params=_params(("parallel",)),
    )(a, da)


def _attn_bwd_call(proj, bias, da, lse, delta, group, dproj):
    S = proj.shape[0]
    r, nb, hp, per = _attn_plan(S, group)
    scale = HEAD_DIM ** -0.5
    kinds = ("q", "qn", "kp", "kc", "vp", "vc", "da", "dan", "lse", "lsen", "dl", "dln") if nb > 1 else ("q", "kc", "vc", "da", "lse", "dl")
    source = dict(da=da, dan=da, lse=lse, lsen=lse, dl=delta, dln=delta)

    per_kind = _refs_per_kind(r, hp)
    per_group = HEADS_PER_GROUP // hp
    block_rows = r * SPAN

    def body(*refs):
        ins = {kind: refs[i * per_kind:(i + 1) * per_kind] for i, kind in enumerate(kinds)}
        b_ref, _, dproj_ref, db_ref, stage_ref, sems = refs[len(kinds) * per_kind:][:6]
        strided_ref = None if r == 1 else refs[-1]
        jg, n = pl.program_id(0), pl.program_id(1)
        prev_ok, cur_ok, next_ok = _band_masks(n, nb)

        @pl.when(n == 0)
        def _():
            db_ref[...] = jnp.zeros_like(db_ref)

        band_ok = _wide_band_mask(n) if nb > 1 else cur_ok

        def fill(slot):
            def put(part, j, rows, value):
                if r == 1:
                    stage_ref[slot, part, :, j * HEAD_DIM:(j + 1) * HEAD_DIM] = value.astype(BF16)
                else:
                    strided_ref[part, j, rows, :] = value

            _for_residues(r, per, functools.partial(residue, put))
            if r > 1:
                for part in range(3):
                    for j in range(hp):
                        for t0 in range(0, block_rows, ROW_TILE):
                            stage_ref[slot, part, t0:t0 + ROW_TILE, j * HEAD_DIM:(j + 1) * HEAD_DIM] = (
                                strided_ref[part, j, t0:t0 + ROW_TILE, :].astype(BF16))

        def residue(put, rho):
            rows = _residue_rows(rho, r)
            for j in range(hp):
                get = lambda kind: _head_rows(ins[kind], j, rows, r)
                q = get("q").astype(BF16)
                kc = get("kc").astype(BF16)
                vc = get("vc").astype(BF16)
                dav = get("da").astype(BF16)
                lse_q, dl_q = get("lse"), get("dl")
                if nb == 1:
                    pc = jnp.exp(jnp.where(cur_ok, _dot_nt(q, kc) * scale + b_ref[j, :, SPAN:], NEG_INF) - lse_q)
                    dsc = pc * (_dot_nt(dav, vc) - dl_q)
                    dsc_b = dsc.astype(BF16)
                    dq = _dot_nn(dsc_b, kc)
                    dk = _dot_tn(dsc_b, q)
                    dv = _dot_tn(pc.astype(BF16), dav)
                    db_ref[j, :, SPAN:] += dsc
                else:
                    qn = get("qn").astype(BF16)
                    dan = get("dan").astype(BF16)
                    keys = jnp.concatenate([get("kp").astype(BF16), kc], axis=0)
                    vals = jnp.concatenate([get("vp").astype(BF16), vc], axis=0)
                    wide = lambda t: jnp.concatenate([t, t], axis=1)
                    p = jnp.exp(jnp.where(band_ok, _dot_nt(q, keys) * scale + b_ref[j], NEG_INF) - wide(lse_q))
                    ds = p * (_dot_nt(dav, vals) - wide(dl_q))
                    dq = _dot_nn(ds.astype(BF16), keys)
                    db_ref[j] += ds
                    pn = jnp.exp(jnp.where(next_ok, _dot_nt(qn, kc) * scale + b_ref[j, :, :SPAN], NEG_INF) - get("lsen"))
                    dsn = pn * (_dot_nt(dan, vc) - get("dln"))
                    both = lambda cur_part, next_part: jnp.concatenate([cur_part.astype(BF16), next_part.astype(BF16)], axis=0)
                    dk = _dot_tn(both(ds[:, SPAN:], dsn), jnp.concatenate([q, qn], axis=0))
                    dv = _dot_tn(both(p[:, SPAN:], pn), jnp.concatenate([dav, dan], axis=0))
                put(0, j, rows, dq * scale)
                put(1, j, rows, dk * scale)
                put(2, j, rows, dv)

        cols = [(part * N_GROUPS + group) * GROUP_WIDTH + jg * (hp * HEAD_DIM) for part in range(3)]
        rows = pl.ds(pl.multiple_of(n * block_rows, SPAN), block_rows)
        _staged_window_stores(stage_ref, dproj_ref, sems, jg * nb + n, per_group * nb, rows, cols, fill)

    band = (hp, SPAN, 2 * SPAN)
    in_specs = [_head_spec(r, nb, hp, kind, group, jj) for kind in kinds for jj in range(per_kind)]
    in_specs += [pl.BlockSpec(band, lambda j, n: (group * per_group + j, 0, 0)), ANY]
    operands = [source.get(kind, proj) for kind in kinds for _ in range(per_kind)] + [bias, dproj]
    scratch = [pltpu.VMEM((2, 3, block_rows, hp * HEAD_DIM), BF16), pltpu.SemaphoreType.DMA((2, 3))]
    if r > 1:
        scratch.append(pltpu.VMEM((3, hp, block_rows, HEAD_DIM), F32))
    return pl.pallas_call(
        body, name=f"attn_bwd_g{group}", grid=(per_group, nb),
        in_specs=in_specs,
        out_specs=[ANY, pl.BlockSpec(band, lambda j, n: (j, 0, 0))],
        out_shape=[jax.ShapeDtypeStruct(dproj.shape, BF16), jax.ShapeDtypeStruct((HEADS_PER_GROUP, SPAN, 2 * SPAN), F32)],
        input_output_aliases={len(operands) - 1: 0},
        scratch_shapes=scratch,
        compiler_params=_params(("arbitrary", "arbitrary"), VMEM_LIMIT),
    )(*operands)


def _tap_rows(xpad_ref, t0, k, width, pad):
    return xpad_ref[pl.ds(t0 + (pad - (width - 1 - k)), TIME_BLOCK), :]


def _conv_block(xpad_ref, t0, w_ref, width, pad):
    acc = None
    for k in range(width):
        term = w_ref[k:k + 1, :] * _tap_rows(xpad_ref, t0, k, width, pad)
        acc = term if acc is None else acc + term
    return acc


def _conv_transpose_block(dpad_ref, t0, w_ref, width):
    acc = None
    for k in range(width):
        term = w_ref[k:k + 1, :] * dpad_ref[pl.ds(t0 + (width - 1 - k), TIME_BLOCK), :]
        acc = term if acc is None else acc + term
    return acc


def _conv_weight_grad(xpad_ref, t0, dy, dw_ref, width, pad):
    for k in range(width):
        dw_ref[k:k + 1, :] += jnp.sum(dy * _tap_rows(xpad_ref, t0, k, width, pad), axis=0, keepdims=True)


def _time_loop(S, step, skip_first=0, skip_last=0):
    def it(tb, carry):
        step(pl.multiple_of(tb * TIME_BLOCK, TIME_BLOCK))
        return carry

    lax.fori_loop(skip_first, S // TIME_BLOCK - skip_last, it, 0)


def _fill_head(head_ref, x_ref, pad):
    head_ref[0:pad, :] = jnp.zeros((pad, LANES), F32)
    head_ref[pad:, :] = x_ref[0:TIME_BLOCK, :]


def _fill_tail(tail_ref, x_ref, pad):
    S = x_ref.shape[0]
    tail_ref[0:TIME_BLOCK, :] = x_ref[S - TIME_BLOCK:S, :]
    tail_ref[TIME_BLOCK:, :] = jnp.zeros((pad, LANES), F32)


def _conv_fwd_call(proj, col0, w, b):
    S = proj.shape[0]
    C = w.shape[1]
    nt = C // LANES
    v0, g0 = col0 // LANES, (col0 + C) // LANES

    def body(val_ref, gate_ref, w_ref, b_ref, o_ref, pad_ref):
        pad_ref[0:CONV_PAD, :] = jnp.zeros((CONV_PAD, LANES), F32)
        pad_ref[CONV_PAD:, :] = val_ref[...] * _sigmoid(gate_ref[...])

        def step(t0):
            o_ref[pl.ds(t0, TIME_BLOCK), :] = _conv_block(pad_ref, t0, w_ref, CONV_WIDTH, CONV_PAD) + b_ref[...]

        _time_loop(S, step)

    seq = lambda off: pl.BlockSpec((S, LANES), lambda i: (0, off + i))
    return pl.pallas_call(
        body, name="conv_module", grid=(nt,),
        in_specs=[seq(v0), seq(g0), pl.BlockSpec((CONV_WIDTH, LANES), lambda i: (0, i)), pl.BlockSpec((1, LANES), lambda i: (0, i))],
        out_specs=seq(0), out_shape=jax.ShapeDtypeStruct((S, C), F32),
        scratch_shapes=[pltpu.VMEM((S + CONV_PAD, LANES), F32)],
        compiler_params=_params(("parallel",)),
    )(proj, proj, w, b)


def _conv_bwd_call(proj, col0, w, dc1, dproj):
    S = proj.shape[0]
    C = w.shape[1]
    nt = C // LANES
    v0, g0 = col0 // LANES, (col0 + C) // LANES

    def body(val_ref, gate_ref, w_ref, dy_ref, _, dproj_ref, dw_ref, db_ref, xpad_ref, tail_ref, dwacc_ref, stage_ref, sems):
        i = pl.program_id(0)
        xpad_ref[0:CONV_PAD, :] = jnp.zeros((CONV_PAD, LANES), F32)
        xpad_ref[CONV_PAD:, :] = val_ref[...] * _sigmoid(gate_ref[...])
        _fill_tail(tail_ref, dy_ref, CONV_PAD)
        dwacc_ref[...] = jnp.zeros_like(dwacc_ref)

        def fill(slot):
            def block(t0, dy_src, dy_t0):
                rows = pl.ds(t0, TIME_BLOCK)
                _conv_weight_grad(xpad_ref, t0, dy_ref[rows, :], dwacc_ref, CONV_WIDTH, CONV_PAD)
                dc0 = _conv_transpose_block(dy_src, dy_t0, w_ref, CONV_WIDTH)
                sg = _sigmoid(gate_ref[rows, :])
                stage_ref[slot, 0, rows, :] = (dc0 * sg).astype(BF16)
                stage_ref[slot, 1, rows, :] = (dc0 * val_ref[rows, :] * sg * (1.0 - sg)).astype(BF16)

            _time_loop(S, lambda t0: block(t0, dy_ref, t0), skip_last=1)
            block(S - TIME_BLOCK, tail_ref, 0)

        _staged_window_stores(stage_ref, dproj_ref, sems, i, nt, pl.ds(0, S),
                              [col0 + i * LANES, col0 + C + i * LANES], fill)
        dw_ref[...] = dwacc_ref[...]
        db_ref[...] = jnp.sum(dy_ref[...], axis=0, keepdims=True)

    seq = lambda off: pl.BlockSpec((S, LANES), lambda i: (0, off + i))
    return pl.pallas_call(
        body, name="conv_module_bwd", grid=(nt,),
        in_specs=[seq(v0), seq(g0), pl.BlockSpec((CONV_WIDTH, LANES), lambda i: (0, i)), seq(0), ANY],
        out_specs=[ANY, pl.BlockSpec((CONV_PAD, LANES), lambda i: (0, i)), pl.BlockSpec((1, LANES), lambda i: (0, i))],
        out_shape=[jax.ShapeDtypeStruct(dproj.shape, BF16),
                   jax.ShapeDtypeStruct((CONV_PAD, C), F32), jax.ShapeDtypeStruct((1, C), F32)],
        input_output_aliases={4: 0},
        scratch_shapes=[pltpu.VMEM((S + CONV_PAD, LANES), F32), pltpu.VMEM((TIME_BLOCK + CONV_PAD, LANES), F32),
                        pltpu.VMEM((CONV_PAD, LANES), F32),
                        pltpu.VMEM((2, 2, S, LANES), BF16), pltpu.SemaphoreType.DMA((2, 2))],
        compiler_params=_params(("arbitrary",)),
    )(proj, proj, w, dc1, dproj)


def _ffn_fwd_call(u, w, b):
    S, C2 = u.shape
    C = C2 // 2
    nt = C // LANES

    def body(ug_ref, uv_ref, wg_ref, wv_ref, bg_ref, bv_ref, f_ref, hg_ref, hv_ref):
        _fill_head(hg_ref, ug_ref, FFN_PAD)
        _fill_head(hv_ref, uv_ref, FFN_PAD)

        def block(t0, xg_ref, xv_ref, x_t0, pad):
            cg = _conv_block(xg_ref, x_t0, wg_ref, FFN_CONV_WIDTH, pad) + bg_ref[...]
            cv = _conv_block(xv_ref, x_t0, wv_ref, FFN_CONV_WIDTH, pad) + bv_ref[...]
            f_ref[pl.ds(t0, TIME_BLOCK), :] = (_gelu(cg) * cv).astype(BF16)

        block(0, hg_ref, hv_ref, 0, FFN_PAD)
        _time_loop(S, lambda t0: block(t0, ug_ref, uv_ref, t0, 0), skip_first=1)

    seq = lambda off: pl.BlockSpec((S, LANES), lambda i: (0, off + i))
    wsp = lambda off: pl.BlockSpec((FFN_CONV_WIDTH, LANES), lambda i: (0, off + i))
    bsp = lambda off: pl.BlockSpec((1, LANES), lambda i: (0, off + i))
    return pl.pallas_call(
        body, name="ffn_conv_geglu", grid=(nt,),
        in_specs=[seq(0), seq(nt), wsp(0), wsp(nt), bsp(0), bsp(nt)],
        out_specs=seq(0), out_shape=jax.ShapeDtypeStruct((S, C), BF16),
        scratch_shapes=[pltpu.VMEM((FFN_PAD + TIME_BLOCK, LANES), F32)] * 2,
        compiler_params=_params(("parallel",)),
    )(u, u, w, w, b, b)


def _ffn_bwd_call(u, w, b, df):
    S, C2 = u.shape
    C = C2 // 2
    nt = C // LANES

    def body(ug_ref, uv_ref, wg_ref, wv_ref, bg_ref, bv_ref, df_ref,
             du_ref, dwg_ref, dwv_ref, dbg_ref, dbv_ref,
             hg_ref, hv_ref, dg_ref, dv_ref, dwg_acc, dwv_acc, dbg_acc, dbv_acc):
        zeros = jnp.zeros((FFN_PAD, LANES), F32)
        _fill_head(hg_ref, ug_ref, FFN_PAD)
        _fill_head(hv_ref, uv_ref, FFN_PAD)
        dg_ref[S:, :] = zeros
        dv_ref[S:, :] = zeros
        dwg_acc[...] = jnp.zeros_like(dwg_acc)
        dwv_acc[...] = jnp.zeros_like(dwv_acc)
        dbg_acc[...] = jnp.zeros_like(dbg_acc)
        dbv_acc[...] = jnp.zeros_like(dbv_acc)

        def first(t0, xg_ref, xv_ref, x_t0, pad):
            rows = pl.ds(t0, TIME_BLOCK)
            cg = _conv_block(xg_ref, x_t0, wg_ref, FFN_CONV_WIDTH, pad) + bg_ref[...]
            cv = _conv_block(xv_ref, x_t0, wv_ref, FFN_CONV_WIDTH, pad) + bv_ref[...]
            dfb = df_ref[rows, :]
            gelu, gelu_grad = _gelu_and_grad(cg)
            dcg = dfb * cv * gelu_grad
            dcv = dfb * gelu
            dg_ref[rows, :] = dcg
            dv_ref[rows, :] = dcv
            _conv_weight_grad(xg_ref, x_t0, dcg, dwg_acc, FFN_CONV_WIDTH, pad)
            _conv_weight_grad(xv_ref, x_t0, dcv, dwv_acc, FFN_CONV_WIDTH, pad)
            dbg_acc[...] += jnp.sum(dcg, axis=0, keepdims=True)
            dbv_acc[...] += jnp.sum(dcv, axis=0, keepdims=True)

        def second(t0):
            rows = pl.ds(t0, TIME_BLOCK)
            du_ref[0, rows, :] = _conv_transpose_block(dg_ref, t0, wg_ref, FFN_CONV_WIDTH).astype(BF16)
            du_ref[1, rows, :] = _conv_transpose_block(dv_ref, t0, wv_ref, FFN_CONV_WIDTH).astype(BF16)

        first(0, hg_ref, hv_ref, 0, FFN_PAD)
        _time_loop(S, lambda t0: first(t0, ug_ref, uv_ref, t0, 0), skip_first=1)
        _time_loop(S, second)
        dwg_ref[...] = dwg_acc[...]
        dwv_ref[...] = dwv_acc[...]
        dbg_ref[...] = dbg_acc[...]
        dbv_ref[...] = dbv_acc[...]

    seq = lambda off: pl.BlockSpec((S, LANES), lambda i: (0, off + i))
    wsp = lambda off: pl.BlockSpec((FFN_CONV_WIDTH, LANES), lambda i: (0, off + i))
    bsp = lambda off: pl.BlockSpec((1, LANES), lambda i: (0, off + i))
    return pl.pallas_call(
        body, name="ffn_conv_geglu_bwd", grid=(nt,),
        in_specs=[seq(0), seq(nt), wsp(0), wsp(nt), bsp(0), bsp(nt), seq(0)],
        out_specs=[pl.BlockSpec((2, S, LANES), lambda i: (0, 0, i)),
                   pl.BlockSpec((SUBLANES, LANES), lambda i: (0, i)), pl.BlockSpec((SUBLANES, LANES), lambda i: (0, i)),
                   bsp(0), bsp(0)],
        out_shape=[jax.ShapeDtypeStruct((2, S, C), BF16)] + [jax.ShapeDtypeStruct((SUBLANES, C), F32)] * 2
        + [jax.ShapeDtypeStruct((1, C), F32)] * 2,
        scratch_shapes=[pltpu.VMEM((FFN_PAD + TIME_BLOCK, LANES), F32)] * 2 + [pltpu.VMEM((S + FFN_PAD, LANES), F32)] * 2
        + [pltpu.VMEM((SUBLANES, LANES), F32)] * 2
        + [pltpu.VMEM((1, LANES), F32)] * 2,
        compiler_params=_params(("parallel",)),
    )(u, u, w, w, b, b, df)


def _adamw(w_ref, g_ref, m_ref, v_ref, d_ref, mo_ref, vo_ref):
    gv = g_ref[...]
    mn = ADAM_B1 * m_ref[...] + (1.0 - ADAM_B1) * gv
    vn = ADAM_B2 * v_ref[...] + (1.0 - ADAM_B2) * (gv * gv)
    mo_ref[...] = mn
    vo_ref[...] = vn
    m_hat = mn * (1.0 / (1.0 - ADAM_B1 ** ADAM_STEP))
    v_hat = vn * (1.0 / (1.0 - ADAM_B2 ** ADAM_STEP))
    d_ref[...] = -ADAM_LR * (m_hat / (jnp.sqrt(v_hat) + ADAM_EPS) + ADAM_WD * w_ref[...])


def _adamw_call(w, g, m, v, name):
    R, C = w.shape
    tr = _row_tile(R, C)

    def body(w_ref, g_ref, m_ref, v_ref, go_ref, d_ref, mo_ref, vo_ref):
        go_ref[...] = g_ref[...]
        _adamw(w_ref, g_ref, m_ref, v_ref, d_ref, mo_ref, vo_ref)

    spec = pl.BlockSpec((tr, C), lambda i: (i, 0))
    return pl.pallas_call(
        body, name=name, grid=(R // tr,),
        in_specs=[spec] * 4, out_specs=[spec] * 4,
        out_shape=[jax.ShapeDtypeStruct((R, C), F32)] * 4,
        compiler_params=_params(("parallel",)),
    )(w, g, m, v)


def _adamw_small_call(ws, gs, ms, vs):
    n = len(ws)

    def body(*refs):
        w_refs, g_refs, m_refs, v_refs, d_refs, mo_refs, vo_refs = (refs[i * n:(i + 1) * n] for i in range(7))
        for i in range(n):
            _adamw(w_refs[i], g_refs[i], m_refs[i], v_refs[i], d_refs[i], mo_refs[i], vo_refs[i])

    whole = pl.BlockSpec(memory_space=pltpu.VMEM)
    outs = pl.pallas_call(
        body, name="adamw_small",
        in_specs=[whole] * (4 * n), out_specs=[whole] * (3 * n),
        out_shape=[jax.ShapeDtypeStruct(w.shape, F32) for w in ws] * 3,
    )(*ws, *gs, *ms, *vs)
    return outs[:n], outs[n:2 * n], outs[2 * n:]


def _position():
    return lax.axis_index("x"), lax.axis_index("y"), lax.axis_index("c")


def _chip_peers(x, y):
    return [(x, 1 - y), (1 - x, y), (1 - x, 1 - y)]


def _half_rows(ref, core, rows):
    h = rows // 2
    start = pl.multiple_of(core * h, PACKED_ROWS)
    return ref.at[pl.ds(start, h), :] if len(ref.shape) == 2 else ref.at[:, pl.ds(start, h), :]


def _shard_half(ref, shard, core, rows):
    h = rows // 2
    return ref.at[shard, pl.ds(pl.multiple_of(core * h, PACKED_ROWS), h), :]


ANY = pl.BlockSpec(memory_space=pl.ANY)


def _first_hop_copies(srcs, lands):
    x, y, c = _position()
    chip = 2 * x + y
    targets = [(px, py, c) for px, py in _chip_peers(x, y)] + [(x, y, 1 - c)]
    rows = srcs[0].shape[0]
    out = []
    for i, (s, l) in enumerate(zip(srcs, lands)):
        for k, dev in enumerate(targets):
            if i == 0 and k < 3:
                out.append((_half_rows(s, c, rows), _shard_half(l, chip, c, rows), dev, k))
            else:
                out.append((s, l.at[chip], dev, len(targets) * i + k))
    return out


def _second_hop_copies(srcs, lands):
    x, y, c = _position()
    rows = lands[0].shape[1]
    out = []
    for k, (px, py) in enumerate(_chip_peers(x, y)):
        half = _shard_half(lands[0], 2 * px + py, c, rows)
        out.append((half, half, (x, y, 1 - c), k))
    return out


HBM_SPEC = pl.BlockSpec(memory_space=pltpu.HBM)
SEM_SPEC = pl.BlockSpec(memory_space=pltpu.SEMAPHORE)
DATAFLOW = pltpu.SideEffectType.DATAFLOW_SIDE_EFFECTING


def _in_hbm(a):
    return pltpu.with_memory_space_constraint(a, pltpu.HBM)


def _split_start(name, groups, after, carry=None):
    spans, arrays = [], []
    for srcs, lands, _, _ in groups:
        spans.append((len(arrays), len(srcs), len(lands)))
        arrays += list(srcs) + list(lands)
    if carry is not None:
        arrays.append(carry)
    na, ng = len(arrays), len(groups)

    def body(*refs):
        sems, token = refs[na + 1:na + 1 + 2 * ng], refs[-1]
        for g, (_, _, _, copies) in enumerate(groups):
            off, ns, nl = spans[g]
            for src, dst, dev, idx in copies(refs[off:off + ns], refs[off + ns:off + ns + nl]):
                pltpu.make_async_remote_copy(src_ref=src, dst_ref=dst, send_sem=sems[2 * g].at[idx], recv_sem=sems[2 * g + 1].at[idx],
                                             device_id=dev, device_id_type=MESH).start()
        token[...] = jnp.zeros_like(token)

    outs = pl.pallas_call(
        body, name=name,
        in_specs=[HBM_SPEC] * na + [ANY],
        out_specs=[SEM_SPEC] * (2 * ng) + [HBM_SPEC] * na + [pl.BlockSpec(memory_space=pltpu.VMEM)],
        out_shape=[pltpu.SemaphoreType.DMA((n_sems,)) for _, _, n_sems, _ in groups for _ in range(2)]
        + [pltpu.HBM(a.shape, a.dtype) for a in arrays] + [jax.ShapeDtypeStruct((SUBLANES, LANES), F32)],
        input_output_aliases={i: 2 * ng + i for i in range(na)},
        compiler_params=pltpu.CompilerParams(has_side_effects=DATAFLOW),
    )(*[_in_hbm(a) for a in arrays], after)
    started = []
    for g, (off, ns, nl) in enumerate(spans):
        thru = outs[2 * ng + off:2 * ng + off + ns + nl]
        started.append(dict(send=outs[2 * g], recv=outs[2 * g + 1], srcs=list(thru[:ns]), lands=list(thru[ns:]),
                            tile=outs[-1], token=outs[-1][0, 0], carry=None if carry is None else outs[2 * ng + na - 1]))
    return started


def _split_wait(name, started, copies, after):
    n, m = len(started["srcs"]), len(started["lands"])
    after = list(after) if isinstance(after, (list, tuple)) else [after]

    def body(*refs):
        src_refs, land_refs = refs[:n], refs[n:n + m]
        send_sem, recv_sem = refs[n + m], refs[n + m + 1]
        for src, dst, dev, idx in copies(src_refs, land_refs):
            cp = pltpu.make_async_remote_copy(src_ref=src, dst_ref=dst, send_sem=send_sem.at[idx], recv_sem=recv_sem.at[idx],
                                              device_id=dev, device_id_type=MESH)
            cp.wait_send()
            cp.wait_recv()

    arrays = started["srcs"] + started["lands"]
    outs = pl.pallas_call(
        body, name=name,
        in_specs=[HBM_SPEC] * (n + m) + [SEM_SPEC, SEM_SPEC] + [ANY] * len(after),
        out_specs=[HBM_SPEC] * (n + m),
        out_shape=[pltpu.HBM(a.shape, a.dtype) for a in arrays],
        input_output_aliases={i: i for i in range(n + m)},
        compiler_params=pltpu.CompilerParams(has_side_effects=DATAFLOW),
    )(*arrays, started["send"], started["recv"], *after)
    return list(outs)


def _gather_copies(srcs, lands):
    x, y, c = _position()
    chip = 2 * x + y
    targets = [(px, py, c) for px, py in _chip_peers(x, y)] + [(x, y, 1 - c)]
    return [(s, l.at[chip], dev, len(targets) * i + k) for i, (s, l) in enumerate(zip(srcs, lands)) for k, dev in enumerate(targets)]


def _sibling_copies(srcs, lands):
    x, y, c = _position()
    return [(_half_rows(srcs[0], 1 - c, srcs[0].shape[1]), lands[0], (x, y, 1 - c), 0)]


def _sibling_whole_copies(srcs, lands):
    x, y, c = _position()
    return [(srcs[0], lands[0], (x, y, 1 - c), 0)]


def _exchange_copies(srcs, lands):
    x, y, c = _position()
    return [(srcs[0].at[2 * px + py], lands[0].at[k], (px, py, c), k) for k, (px, py) in enumerate(_chip_peers(x, y))]


def _pair_sum_call(grad, recv, chip_core, name):
    _, h, B = recv.shape
    tr = _row_tile(h, B)

    def body(cc_ref, g_ref, r_ref, o_ref, ob_ref):
        s = g_ref[...] + r_ref[...]
        ob_ref[...] = s.astype(BF16)

        @pl.when(pl.program_id(1) == cc_ref[0])
        def _():
            o_ref[...] = s

    g_spec = pl.BlockSpec((None, tr, B), lambda i, q, cc_ref: (q, cc_ref[1] * (h // tr) + i, 0))
    spec = pl.BlockSpec((None, tr, B), lambda i, q, cc_ref: (q, i, 0))
    own_spec = pl.BlockSpec((tr, B), lambda i, q, cc_ref: (i, 0))
    return pl.pallas_call(
        body, name=name,
        grid_spec=pltpu.PrefetchScalarGridSpec(num_scalar_prefetch=1, grid=(h // tr, N_CHIPS), in_specs=[g_spec, spec],
                                               out_specs=[own_spec, spec]),
        out_shape=[jax.ShapeDtypeStruct((h, B), F32), jax.ShapeDtypeStruct(recv.shape, BF16)],
        compiler_params=_params(("parallel", "arbitrary")),
    )(chip_core, grad, recv)


def _chip_sum_call(partial, recv, chip_core, name):
    _, h, B = recv.shape
    tr = _row_tile(h, B)

    def body(cc_ref, p_ref, r_ref, o_ref):
        o_ref[...] = ((p_ref[...] + r_ref[0].astype(F32)) + r_ref[1].astype(F32)) + r_ref[2].astype(F32)

    return pl.pallas_call(
        body, name=name,
        grid_spec=pltpu.PrefetchScalarGridSpec(
            num_scalar_prefetch=1, grid=(h // tr,),
            in_specs=[pl.BlockSpec((tr, B), lambda i, cc_ref: (i, 0)),
                      pl.BlockSpec((3, tr, B), lambda i, cc_ref: (0, i, 0))],
            out_specs=pl.BlockSpec((tr, B), lambda i, cc_ref: (cc_ref[1] * (h // tr) + i, 0))),
        out_shape=jax.ShapeDtypeStruct((2 * h, B), F32),
        compiler_params=_params(("parallel",)),
    )(chip_core, partial, recv)


def _assemble_copies(srcs, lands):
    x, y, c = _position()
    out = []
    for i, land in enumerate(lands):
        half = _half_rows(land, c, land.shape[0])
        out.append((half, half, (x, y, 1 - c), i))
    return out


N_DEVICES = 8


def _allsum_copies(srcs, lands):
    x, y, c = _position()
    me = 4 * x + 2 * y + c
    out = []
    for k in range(1, N_DEVICES):
        peer = (1 - x if k & 4 else x, 1 - y if k & 2 else y, 1 - c if k & 1 else c)
        out.append((srcs[0], lands[0].at[me], peer, k - 1))
    return out


def _ordered_sum_call(mine, landed, me_chip, shapes, sharded_cols):
    rows = mine.shape[0]
    outs = [(s[0], n) if n else s for s, n in zip(shapes, sharded_cols)]

    def body(mc_ref, x_ref, l_ref, *refs):
        acc_ref = refs[-1]
        acc = jnp.where(mc_ref[0] == 0, x_ref[...], l_ref[0])
        for d in range(1, N_DEVICES):
            acc = acc + jnp.where(mc_ref[0] == d, x_ref[...], l_ref[d])
        acc_ref[...] = acc
        first = 0
        for o_ref, (r, c), n in zip(refs[:-1], shapes, sharded_cols):
            per_row = c // LANES

            def unpack(chip, o_ref=o_ref, r=r, n=n, per_row=per_row, first=first):
                for i in range(r):
                    for j in range((n or per_row * LANES) // LANES):
                        src = first + i * per_row + chip * ((n or 0) // LANES) + j
                        o_ref[i:i + 1, j * LANES:(j + 1) * LANES] = acc_ref[src:src + 1, :]

            if n:
                for q in range(N_CHIPS):
                    pl.when(mc_ref[1] == q)(functools.partial(unpack, q))
            else:
                unpack(0)
            first += r * per_row

    results = pl.pallas_call(
        body, name="small_grad_sum",
        in_specs=[pl.BlockSpec(memory_space=pltpu.SMEM), pl.BlockSpec(memory_space=pltpu.VMEM), pl.BlockSpec(memory_space=pltpu.VMEM)],
        out_specs=[pl.BlockSpec(memory_space=pltpu.VMEM)] * len(outs),
        out_shape=[jax.ShapeDtypeStruct(s, F32) for s in outs],
        scratch_shapes=[pltpu.VMEM((rows, LANES), F32)],
    )(me_chip, mine, landed)
    return results


def _pack(arrays):
    flat = jnp.concatenate([a.reshape(-1).astype(F32) for a in arrays])
    rows = -(-flat.shape[0] // LANES)
    rows = -(-rows // SUBLANES) * SUBLANES
    flat = jnp.pad(flat, (0, rows * LANES - flat.shape[0]))
    return flat.reshape(rows, LANES)


def _local_step(xs, target, P, late_weights, on_grad):
    S, D = xs.shape
    qkv_width = 3 * N_HEADS * HEAD_DIM
    glu_col0, gate_col0 = qkv_width, qkv_width + 2 * D
    shard_major = lambda g: g.reshape(N_CHIPS, g.shape[0] // N_CHIPS, g.shape[1])

    h1 = _rms_fwd_call(xs, P["norm_mix_pre"])
    buckets = _bucket_tables()
    bias = _bias_table_call(P["rel_bias"] + 0.0 * h1[0, 0].astype(F32), buckets)
    P = dict(P, **late_weights("in", bias))
    proj = _matmul(h1, P["w_in"], "nn", "proj_in")
    parts = []
    for g in range(N_GROUPS):
        parts += _attn_fwd_call(proj, bias, g)
    a, a_bf, lse = _attn_merge_call(parts)
    P = dict(P, **late_weights("mix", a_bf))
    y_a = _matmul(a_bf, P["w_attn_out"], "nn", "attn_out")
    c1 = _conv_fwd_call(proj, glu_col0, P["conv_dw_w"], P["conv_dw_b"])
    cact = _ln_silu_call(c1, P["conv_ln_g"], P["conv_ln_b"])
    y_c = _matmul(cact, P["conv_pw_w"], "nn", "conv_pw")
    mixed = _mix_call(proj, gate_col0, P["b_gate"], y_a, y_c)
    out = _matmul(mixed, P["w_out"], "nn", "mix_out")
    x1, h2 = _res1_call(xs, out, P["norm_mix_post"], P["norm_ffn_pre"])
    P = dict(P, **late_weights("up", h2))
    u = _matmul(h2, P["w_up"], "nn", "ffn_up")
    f = _ffn_fwd_call(u, P["ffn_conv_w"], P["ffn_conv_b"])
    P = dict(P, **late_weights("down", f))
    yff = _matmul(f, P["w_down"], "nn", "ffn_down")
    loss_tile, dx2, dyff, dg_ffn_post = _loss_call(yff, x1, P["norm_ffn_post"], target)

    G = {}
    G["norm_ffn_post"] = dg_ffn_post
    on_grad("w_down", shard_major(_matmul(f, dyff, "tn", "ffn_down_dw")))
    df = _matmul(dyff, P["w_down"], "nt", "ffn_down_dx")
    du, dwg, dwv, dbg, dbv = _ffn_bwd_call(u, P["ffn_conv_w"], P["ffn_conv_b"], df)
    G["ffn_conv_w"] = jnp.concatenate([dwg[:FFN_CONV_WIDTH], dwv[:FFN_CONV_WIDTH]], axis=1)
    G["ffn_conv_b"] = jnp.concatenate([dbg, dbv], axis=1)
    du = on_grad("w_up", functools.partial(_grad_half_matmul, h2, "ffn_up_dw"), carry=du)
    dh2 = _matmul(du, P["w_up"], "nt", "ffn_up_dx")
    dx1, dout, G["norm_ffn_pre"], G["norm_mix_post"] = _mid_bwd_call(x1, P["norm_ffn_pre"], dh2, dx2, out, P["norm_mix_post"])
    on_grad("w_out", shard_major(_matmul(mixed, dout, "tn", "mix_out_dw")))
    dmixed = _matmul(dout, P["w_out"], "nt", "mix_out_dx")
    dya, dyc, dproj, dba, dbc = _mix_bwd_call(dmixed, proj, gate_col0, P["b_gate"], y_a, y_c)
    G["b_gate"] = jnp.concatenate([dba, dbc], axis=1)
    on_grad("w_attn_out", _matmul(a_bf, dya, "tn", "attn_out_dw", out_shards=True))
    dyc = on_grad("conv_pw_w", shard_major(_matmul(cact, dyc, "tn", "conv_pw_dw")), carry=dyc)
    da = _matmul(dya, P["w_attn_out"], "nt", "attn_out_dx")
    dcact = _matmul(dyc, P["conv_pw_w"], "nt", "conv_pw_dx")
    dc1, G["conv_ln_g"], G["conv_ln_b"] = _ln_silu_bwd_call(c1, P["conv_ln_g"], P["conv_ln_b"], dcact)
    dproj, dw_dw, G["conv_dw_b"] = _conv_bwd_call(proj, glu_col0, P["conv_dw_w"], dc1, dproj)
    G["conv_dw_w"] = dw_dw[:CONV_WIDTH]
    delta = _attn_delta_call(a, da)
    dbs = []
    for g in range(N_GROUPS):
        dproj, db = _attn_bwd_call(proj, bias, da, lse, delta, g, dproj)
        dbs.append(db)
    G["rel_bias"] = _bias_grad_call(jnp.concatenate(dbs, axis=0), buckets)
    dproj = on_grad("w_in", functools.partial(_grad_half_matmul, h1, "proj_in_dw"), carry=dproj)
    dh1 = _matmul(dproj, P["w_in"], "nt", "proj_in_dx")
    dh1 = on_grad(None, None, carry=dh1)
    grad_x, G["norm_mix_pre"] = _in_bwd_call(xs, P["norm_mix_pre"], dh1, dx1)
    return loss_tile, grad_x, G


def kernel(x, w_in, b_gate, rel_bias, w_attn_out, conv_dw_w, conv_dw_b, conv_ln_g, conv_ln_b, conv_pw_w, w_out, norm_mix_pre, norm_mix_post, norm_ffn_pre, norm_ffn_post, w_up, ffn_conv_w, ffn_conv_b, w_down, loss_target, m_w_in, m_b_gate, m_rel_bias, m_w_attn_out, m_conv_dw_w, m_conv_dw_b, m_conv_ln_g, m_conv_ln_b, m_conv_pw_w, m_w_out, m_norm_mix_pre, m_norm_mix_post, m_norm_ffn_pre, m_norm_ffn_post, m_w_up, m_ffn_conv_w, m_ffn_conv_b, m_w_down, v_w_in, v_b_gate, v_rel_bias, v_w_attn_out, v_conv_dw_w, v_conv_dw_b, v_conv_ln_g, v_conv_ln_b, v_conv_pw_w, v_w_out, v_norm_mix_pre, v_norm_mix_post, v_norm_ffn_pre, v_norm_ffn_post, v_w_up, v_ffn_conv_w, v_ffn_conv_b, v_w_down):
    weights = dict(w_in=w_in, b_gate=b_gate, rel_bias=rel_bias, w_attn_out=w_attn_out, conv_dw_w=conv_dw_w, conv_dw_b=conv_dw_b,
                   conv_ln_g=conv_ln_g, conv_ln_b=conv_ln_b, conv_pw_w=conv_pw_w, w_out=w_out, norm_mix_pre=norm_mix_pre,
                   norm_mix_post=norm_mix_post, norm_ffn_pre=norm_ffn_pre, norm_ffn_post=norm_ffn_post, w_up=w_up,
                   ffn_conv_w=ffn_conv_w, ffn_conv_b=ffn_conv_b, w_down=w_down)
    m_in = dict(w_in=m_w_in, b_gate=m_b_gate, rel_bias=m_rel_bias, w_attn_out=m_w_attn_out, conv_dw_w=m_conv_dw_w,
                conv_dw_b=m_conv_dw_b, conv_ln_g=m_conv_ln_g, conv_ln_b=m_conv_ln_b, conv_pw_w=m_conv_pw_w, w_out=m_w_out,
                norm_mix_pre=m_norm_mix_pre, norm_mix_post=m_norm_mix_post, norm_ffn_pre=m_norm_ffn_pre,
                norm_ffn_post=m_norm_ffn_post, w_up=m_w_up, ffn_conv_w=m_ffn_conv_w, ffn_conv_b=m_ffn_conv_b, w_down=m_w_down)
    v_in = dict(w_in=v_w_in, b_gate=v_b_gate, rel_bias=v_rel_bias, w_attn_out=v_w_attn_out, conv_dw_w=v_conv_dw_w,
                conv_dw_b=v_conv_dw_b, conv_ln_g=v_conv_ln_g, conv_ln_b=v_conv_ln_b, conv_pw_w=v_conv_pw_w, w_out=v_w_out,
                norm_mix_pre=v_norm_mix_pre, norm_mix_post=v_norm_mix_post, norm_ffn_pre=v_norm_ffn_pre,
                norm_ffn_post=v_norm_ffn_post, w_up=v_w_up, ffn_conv_w=v_ffn_conv_w, ffn_conv_b=v_ffn_conv_b, w_down=v_w_down)
    names = list(weights)
    xi, yi, ci = _position()
    chip = 2 * xi + yi
    core_arr = jnp.reshape(ci, (1,)).astype(jnp.int32)

    xs = x[0]
    target = loss_target[0]
    S, D = xs.shape

    big = ["w_in", "w_attn_out", "conv_pw_w", "w_out", "w_up", "w_down"]
    row_sharded = ("conv_pw_w", "w_out", "w_down")
    natural = lambda k, g: g.reshape(-1, g.shape[2]) if k in row_sharded else g
    first_srcs = [w_in[0].astype(BF16), conv_dw_w[0], ffn_conv_w[0]]
    first_lands = [lax.empty((N_CHIPS,) + s.shape, s.dtype) for s in first_srcs]
    (first_hop,) = _split_start("gather_in_start", [(first_srcs, first_lands, 4 * len(first_srcs), _first_hop_copies)], core_arr)
    launched = first_hop["token"]
    late_sets = dict(mix=["w_attn_out", "conv_pw_w", "w_out"], up=["w_up"], down=["w_down"])
    late_groups = []
    for keys in late_sets.values():
        srcs = [(weights[k][0] + launched).astype(BF16) for k in keys]
        late_groups.append((srcs, [lax.empty((N_CHIPS,) + s.shape, BF16) for s in srcs], 4 * len(keys), _gather_copies))
    started = {}

    def late_weights(tag, after):
        if tag == "in":
            casts = [s for srcs, _, _, _ in late_groups for s in srcs]
            w_in_halves, dw4, fc4 = _split_wait("gather_in_wait", first_hop, _first_hop_copies, [after] + casts)[len(first_srcs):]
            second_hop, *late = _split_start("gather_in_pass_start", [([], [w_in_halves], 3, _second_hop_copies)] + late_groups, dw4)
            started.update(zip(late_sets, late))
            (w_in_full,) = _split_wait("gather_in_pass_wait", second_hop, _second_hop_copies, second_hop["tile"])
            return dict(w_in=w_in_full, conv_dw_w=jnp.concatenate(list(dw4), axis=1), ffn_conv_w=jnp.concatenate(list(fc4), axis=1))
        landed = _split_wait(f"gather_{tag}_wait", started[tag], _gather_copies, after)[len(late_sets[tag]):]
        return {k: natural(k, g) for k, g in zip(late_sets[tag], landed)}

    chip_core = jnp.stack([chip, ci]).astype(jnp.int32)
    exchanging, pending, second_half = {}, {}, {}

    held = []

    def launch(tag, after, carry=None):
        keys, groups, partial = [], [], {}
        for k in list(exchanging):
            st, copies = exchanging.pop(k)
            gk, r1 = _split_wait(f"sibling_exchange_wait_{k}", st, copies, after)
            if k in second_half:
                partial[k], s16 = second_half.pop(k)(init=r1)
            else:
                partial[k], s16 = _pair_sum_call(gk, r1, chip_core, f"pair_sum_{k}")
            keys.append(k)
            groups.append(([s16], [lax.empty((3,) + s16.shape[1:], BF16)], 3, _exchange_copies))
        fresh = [(k, copies) for k, _, copies in held]
        for _, g3, copies in held:
            rows = g3.shape[1] // 2 if copies is _sibling_copies else g3.shape[1]
            groups.append(([g3], [lax.empty((N_CHIPS, rows, g3.shape[2]), F32)], 1, copies))
        held.clear()
        begun = _split_start(f"grad_exchange_start_{tag}", groups, core_arr, carry)
        for k, st in zip(keys, begun):
            pending[k] = (partial[k], st)
        for (k, copies), st in zip(fresh, begun[len(keys):]):
            exchanging[k] = (st, copies)
        return begun[0]["carry"]

    others = [k for k in big if k != "w_in"]
    assembling = {}

    def on_grad(k, g, carry=None):
        if k is None:
            carried = launch("last", carry[:SUBLANES, :LANES], carry)
            assembling["others"] = assemble_start(others, carried, "others", carried)
            return assembling["others"]["carry"]
        if callable(g):
            theirs = g(jnp.stack([chip, 1 - ci]).astype(jnp.int32), carry)
            held.append((k, theirs, _sibling_whole_copies))
            carried = launch(k, theirs[0, :SUBLANES, :LANES], carry)
            second_half[k] = functools.partial(g, chip_core, carried)
            return carried
        held.append((k, g, _sibling_copies))
        if k in ("w_down", "w_out", "w_attn_out"):
            return carry
        return launch(k, g[0, :SUBLANES, :LANES], carry)

    def assemble_start(keys, after, tag, carry=None):
        halves = []
        for k in keys:
            s32, st = pending[k]
            recv2 = _split_wait(f"chip_exchange_wait_{k}", st, _exchange_copies, after)[1]
            halves.append(_chip_sum_call(s32, recv2, chip_core, f"chip_sum_{k}"))
        (st,) = _split_start(f"grad_assemble_start_{tag}", [([], halves, len(halves), _assemble_copies)], core_arr, carry)
        return st

    def assemble_wait(keys, st, after, tag):
        return dict(zip(keys, _split_wait(f"grad_assemble_wait_{tag}", st, _assemble_copies, after)))

    P = dict(b_gate=b_gate, rel_bias=rel_bias, conv_dw_b=conv_dw_b, conv_ln_g=conv_ln_g, conv_ln_b=conv_ln_b,
             norm_mix_pre=norm_mix_pre + launched, norm_mix_post=norm_mix_post, norm_ffn_pre=norm_ffn_pre,
             norm_ffn_post=norm_ffn_post, ffn_conv_b=ffn_conv_b)
    loss_tile, grad_x, G = _local_step(xs, target, P, late_weights, on_grad)

    small = [k for k in names if k not in big]
    packed = _pack([loss_tile[:1]] + [G[k] for k in small])
    (allsum,) = _split_start("small_grad_allsum_start",
                             [([packed], [jnp.zeros((N_DEVICES,) + packed.shape, F32)], N_DEVICES - 1, _allsum_copies)], core_arr)

    reduced, grads, deltas, new_m, new_v = {}, {}, {}, {}, {}

    def update(keys):
        for k in keys:
            gk, d, mn, vn = _adamw_call(weights[k][0], reduced[k], m_in[k][0], v_in[k][0], f"adamw_{k}")
            grads[k], deltas[k], new_m[k], new_v[k] = gk[None], d[None], mn[None], vn[None]

    reduced.update(assemble_wait(others, assembling["others"], [allsum["tile"], grad_x], "others"))
    update(others)
    assembling["w_in"] = assemble_start(["w_in"], [deltas[k] for k in others], "w_in")

    me_chip = jnp.stack([4 * xi + 2 * yi + ci, chip]).astype(jnp.int32)
    mine, landed = _split_wait("small_grad_allsum_wait", allsum, _allsum_copies, assembling["w_in"]["tile"])
    piece_shapes = [(1, LANES)] + [(G[k].size // LANES, LANES) if k == "rel_bias" else G[k].shape for k in small]
    piece_cols = [0] + [weights[k].shape[2] if k in ("conv_dw_w", "ffn_conv_w") else 0 for k in small]
    loss_row, *summed = _ordered_sum_call(mine, landed, me_chip, piece_shapes, piece_cols)
    loss = loss_row[0, 0]
    for k, gsum in zip(small, summed):
        grads[k] = gsum.reshape(weights[k].shape)
    ds, mns, vns = _adamw_small_call([weights[k] for k in small], [grads[k] for k in small],
                                     [m_in[k] for k in small], [v_in[k] for k in small])
    deltas.update(zip(small, ds))
    new_m.update(zip(small, mns))
    new_v.update(zip(small, vns))
    reduced.update(assemble_wait(["w_in"], assembling["w_in"], list(ds), "w_in"))
    update(["w_in"])

    return (loss, grad_x[None], *[grads[k] for k in names], *[deltas[k] for k in names],
            *[new_m[k] for k in names], *[new_v[k] for k in names])
```

```python
import functools
import math

import jax
import jax.numpy as jnp
import numpy as np
from jax import lax
from jax.experimental import pallas as pl
from jax.experimental.pallas import tpu as pltpu

F32 = jnp.float32
BF16 = jnp.bfloat16
MESH = pl.DeviceIdType.MESH

HEAD_DIM = 128
HEADS_PER_GROUP = 4
DILATED_PATTERNS = ((128, 1), (512, 4), (2048, 16))
N_GROUPS = 3
N_HEADS = N_GROUPS * HEADS_PER_GROUP
SPAN = 128
GROUP_WIDTH = HEADS_PER_GROUP * HEAD_DIM
CONV_WIDTH = 31
FFN_CONV_WIDTH = 3
N_BUCKETS = 32
MAX_DISTANCE = 2048
RMS_EPS = 1e-6
LN_EPS = 1e-5
NEG_INF = -1e30
ADAM_LR = 0.001
ADAM_B1 = 0.9
ADAM_B2 = 0.999
ADAM_EPS = 1e-08
ADAM_WD = 0.01
ADAM_STEP = 10

LANES = 128
SUBLANES = 8
PACKED_ROWS = 16
ROW_TILE = 512
GATE_ROWS, GATE_COLS = 512, 512
TIME_BLOCK = 128
CONV_PAD = 32
FFN_PAD = 8
VMEM_LIMIT = 56 << 20


def _params(sem=None, vmem=None):
    kw = {}
    if sem is not None:
        kw["dimension_semantics"] = sem
    if vmem is not None:
        kw["vmem_limit_bytes"] = vmem
    return pltpu.CompilerParams(**kw)


def _pick(n, cands):
    for c in cands:
        if n % c == 0:
            return c
    return n


ELEMENTWISE_TILE_BYTES = 3 << 19


def _row_tile(rows, cols):
    for align in (16, SUBLANES):
        fits = [t for t in range(align, rows + 1, align) if rows % t == 0 and t * cols * 4 <= ELEMENTWISE_TILE_BYTES]
        if fits:
            return max(fits)
    return SUBLANES


N_CHIPS = 4
M_TILES = (1024, 1408, 512, 256, 128)
N_TILES = (1024, 512, 1408, 256, 128)
K_TILES = (2176, 2048, 1408, 1024, 512, 256, 128)


def _matmul(a, b, mode, name, out_shards=False, tm=None):
    assert a.dtype == BF16 and b.dtype == BF16, (name, a.dtype, b.dtype)
    b3 = b.ndim == 3
    tn = tk = None
    halves = None
    if mode == "nn":
        M, K = a.shape
        N = b.shape[-1] * (N_CHIPS if b3 else 1)
        tn = b.shape[-1] if b3 else None
    elif mode == "nt":
        if a.ndim == 3:
            halves = a.shape[2]
        M, K = a.shape[-2], a.shape[-1] * (a.shape[0] if a.ndim == 3 else 1)
        N = b.shape[-2]
        tk = b.shape[-1] if b3 else None
    else:
        if b3:
            halves = b.shape[2]
        K, M = a.shape
        N = b.shape[-1] * (b.shape[0] if b3 else 1)
        tn = N // N_CHIPS if out_shards else None
    tm = tm or _pick(M, M_TILES)
    tn = tn or _pick(N, N_TILES)
    tk = tk or _pick(K, K_TILES)
    nk = K // tk
    dn = {"nn": (((1,), (0,)), ((), ())), "nt": (((1,), (1,)), ((), ())), "tn": (((0,), (0,)), ((), ()))}[mode]

    def body(a_ref, b_ref, o_ref):
        if nk == 1:
            o_ref[...] = lax.dot_general(a_ref[...], b_ref[...], dn, preferred_element_type=F32)
        else:
            @pl.when(pl.program_id(2) == 0)
            def _():
                o_ref[...] = jnp.zeros_like(o_ref)

            o_ref[...] += lax.dot_general(a_ref[...], b_ref[...], dn, preferred_element_type=F32)

    if mode == "tn":
        a_spec = pl.BlockSpec((tk, tm), lambda i, j, k: (k, i))
    elif halves:
        per = halves // tk
        a_spec = pl.BlockSpec((None, tm, tk), lambda i, j, k: (k // per, i, k % per))
    else:
        a_spec = pl.BlockSpec((tm, tk), lambda i, j, k: (i, k))
    if mode == "nn":
        b_spec = pl.BlockSpec((None, tk, tn), lambda i, j, k: (j, k, 0)) if b3 else pl.BlockSpec((tk, tn), lambda i, j, k: (k, j))
    elif mode == "nt":
        b_spec = pl.BlockSpec((None, tn, tk), lambda i, j, k: (k, j, 0)) if b3 else pl.BlockSpec((tn, tk), lambda i, j, k: (j, k))
    elif halves:
        per = halves // tn
        b_spec = pl.BlockSpec((None, tk, tn), lambda i, j, k: (j // per, k, j % per))
    else:
        b_spec = pl.BlockSpec((tk, tn), lambda i, j, k: (k, j))
    if out_shards:
        out_spec = pl.BlockSpec((None, tm, tn), lambda i, j, k: (j, i, 0))
        out_shape = jax.ShapeDtypeStruct((N_CHIPS, M, tn), F32)
    else:
        out_spec = pl.BlockSpec((tm, tn), lambda i, j, k: (i, j))
        out_shape = jax.ShapeDtypeStruct((M, N), F32)
    return pl.pallas_call(
        body, name=name, grid=(M // tm, N // tn, nk),
        in_specs=[a_spec, b_spec], out_specs=out_spec, out_shape=out_shape,
        compiler_params=_params(("parallel", "parallel", "arbitrary"), VMEM_LIMIT),
    )(a, b)


def _grad_half_matmul(a, name, chip_half, b, init=None):
    K, M = a.shape
    parts = b.ndim == 3
    N = b.shape[-1] * (b.shape[0] if parts else 1)
    h, tn = M // 2, N // N_CHIPS
    summed = init is not None
    dn = (((0,), (0,)), ((), ()))

    def body(ch_ref, a_ref, b_ref, *rest):
        product = lax.dot_general(a_ref[...], b_ref[...], dn, preferred_element_type=F32)
        if not summed:
            rest[0][...] = product
            return
        init_ref, own_ref, sum16_ref = rest
        total = product + init_ref[...]
        sum16_ref[...] = total.astype(BF16)

        @pl.when(pl.program_id(0) == ch_ref[0])
        def _():
            own_ref[...] = total

    if parts:
        per = b.shape[2] // tn
        b_spec = pl.BlockSpec((None, K, tn), lambda j, ch_ref: (j // per, 0, j % per))
    else:
        b_spec = pl.BlockSpec((K, tn), lambda j, ch_ref: (0, j))
    shard_spec = pl.BlockSpec((None, h, tn), lambda j, ch_ref: (j, 0, 0))
    own_spec = pl.BlockSpec((h, tn), lambda j, ch_ref: (0, 0))
    shape = (N_CHIPS, h, tn)
    return pl.pallas_call(
        body, name=name + ("_mine" if summed else "_theirs"),
        grid_spec=pltpu.PrefetchScalarGridSpec(
            num_scalar_prefetch=1, grid=(N_CHIPS,),
            in_specs=[pl.BlockSpec((K, h), lambda j, ch_ref: (0, ch_ref[1])), b_spec] + [shard_spec] * summed,
            out_specs=[own_spec, shard_spec] if summed else shard_spec),
        out_shape=[jax.ShapeDtypeStruct((h, tn), F32), jax.ShapeDtypeStruct(shape, BF16)] if summed
        else jax.ShapeDtypeStruct(shape, F32),
        compiler_params=_params(("arbitrary",), VMEM_LIMIT),
    )(chip_half, a, b, *([init] if summed else []))


def _rms(x, g):
    r = lax.rsqrt(jnp.mean(x * x, axis=-1, keepdims=True) + RMS_EPS)
    return x * r * g


def _rms_bwd(x, g, dy):
    r = lax.rsqrt(jnp.mean(x * x, axis=-1, keepdims=True) + RMS_EPS)
    n = x * r
    dn = dy * g
    dx = r * (dn - n * jnp.mean(dn * n, axis=-1, keepdims=True))
    return dx, jnp.sum(dy * n, axis=0, keepdims=True)


def _sigmoid(x):
    return 1.0 / (1.0 + jnp.exp(-x))


_GELU_C = math.sqrt(2.0 / math.pi)


def _gelu(x):
    return 0.5 * x * (1.0 + jnp.tanh(_GELU_C * (x + 0.044715 * x * x * x)))


def _gelu_and_grad(x):
    x2 = x * x
    t = jnp.tanh(_GELU_C * x * (1.0 + 0.044715 * x2))
    half = 0.5 * (1.0 + t)
    return x * half, half + (0.5 * _GELU_C) * x * (1.0 - t * t) * (1.0 + (3.0 * 0.044715) * x2)


def _row_spec(width, col_block=0):
    return pl.BlockSpec((ROW_TILE, width), lambda i: (i, col_block))


def _vec_spec(width, col_block=0):
    return pl.BlockSpec((1, width), lambda i: (0, col_block))


def _accumulate(ref, part):
    @pl.when(pl.program_id(0) == 0)
    def _():
        ref[...] = part

    @pl.when(pl.program_id(0) > 0)
    def _():
        ref[...] += part


def _rms_fwd_call(x, g):
    S, D = x.shape

    def body(x_ref, g_ref, h_ref):
        h_ref[...] = _rms(x_ref[...], g_ref[...]).astype(BF16)

    return pl.pallas_call(
        body, name="rms_mix_pre", grid=(S // ROW_TILE,),
        in_specs=[_row_spec(D), _vec_spec(D)], out_specs=_row_spec(D),
        out_shape=jax.ShapeDtypeStruct((S, D), BF16),
        compiler_params=_params(("parallel",)),
    )(x, g)


def _ln_silu_call(c1, g, b):
    S, C = c1.shape

    def body(c_ref, g_ref, b_ref, o_ref):
        xv = c_ref[...]
        mu = jnp.mean(xv, axis=-1, keepdims=True)
        xc = xv - mu
        var = jnp.mean(xc * xc, axis=-1, keepdims=True)
        z = xc * lax.rsqrt(var + LN_EPS) * g_ref[...] + b_ref[...]
        o_ref[...] = (z * _sigmoid(z)).astype(BF16)

    return pl.pallas_call(
        body, name="conv_ln_silu", grid=(S // ROW_TILE,),
        in_specs=[_row_spec(C), _vec_spec(C), _vec_spec(C)], out_specs=_row_spec(C),
        out_shape=jax.ShapeDtypeStruct((S, C), BF16),
        compiler_params=_params(("parallel",)),
    )(c1, g, b)


def _ln_silu_bwd_call(c1, g, b, dc):
    S, C = c1.shape

    def body(c_ref, g_ref, b_ref, dc_ref, dx_ref, dg_ref, db_ref):
        xv = c_ref[...]
        mu = jnp.mean(xv, axis=-1, keepdims=True)
        xc = xv - mu
        rs = lax.rsqrt(jnp.mean(xc * xc, axis=-1, keepdims=True) + LN_EPS)
        xh = xc * rs
        z = xh * g_ref[...] + b_ref[...]
        sg = _sigmoid(z)
        dz = dc_ref[...] * (sg * (1.0 + z * (1.0 - sg)))
        dxh = dz * g_ref[...]
        dx_ref[...] = rs * (dxh - jnp.mean(dxh, axis=-1, keepdims=True) - xh * jnp.mean(dxh * xh, axis=-1, keepdims=True))
        _accumulate(dg_ref, jnp.sum(dz * xh, axis=0, keepdims=True))
        _accumulate(db_ref, jnp.sum(dz, axis=0, keepdims=True))

    return pl.pallas_call(
        body, name="conv_ln_silu_bwd", grid=(S // ROW_TILE,),
        in_specs=[_row_spec(C), _vec_spec(C), _vec_spec(C), _row_spec(C)],
        out_specs=[_row_spec(C), _vec_spec(C), _vec_spec(C)],
        out_shape=[jax.ShapeDtypeStruct((S, C), F32), jax.ShapeDtypeStruct((1, C), F32), jax.ShapeDtypeStruct((1, C), F32)],
        compiler_params=_params(("arbitrary",)),
    )(c1, g, b, dc)


def _mix_call(proj, gate_col0, b_gate, y_a, y_c):
    S, D = y_a.shape
    w = GATE_COLS
    nc = D // w
    ga0, gc0 = gate_col0 // w, (gate_col0 + D) // w

    def body(ga_ref, gc_ref, ba_ref, bc_ref, ya_ref, yc_ref, o_ref):
        o_ref[...] = (_sigmoid(ga_ref[...] + ba_ref[...]) * ya_ref[...]
                      + _sigmoid(gc_ref[...] + bc_ref[...]) * yc_ref[...]).astype(BF16)

    tile = lambda off: pl.BlockSpec((GATE_ROWS, w), lambda i, j: (i, off + j))
    vec = lambda off: pl.BlockSpec((1, w), lambda i, j: (0, off + j))
    return pl.pallas_call(
        body, name="gate_mix", grid=(S // GATE_ROWS, nc),
        in_specs=[tile(ga0), tile(gc0), vec(0), vec(nc), tile(0), tile(0)],
        out_specs=tile(0), out_shape=jax.ShapeDtypeStruct((S, D), BF16),
        compiler_params=_params(("parallel", "parallel")),
    )(proj, proj, b_gate, b_gate, y_a, y_c)


def _window_stores(stage_ref, slot, dst_ref, rows, cols, sems):
    width = stage_ref.shape[-1]
    return [pltpu.make_async_copy(stage_ref.at[slot, p], dst_ref.at[rows, pl.ds(pl.multiple_of(c, LANES), width)], sems.at[slot, p])
            for p, c in enumerate(cols)]


def _staged_window_stores(stage_ref, dst_ref, sems, step, n_steps, rows, cols, fill):
    slot = step % 2
    copies = lambda s: _window_stores(stage_ref, s, dst_ref, rows, cols, sems)

    @pl.when(step >= 2)
    def _():
        for cp in copies(slot):
            cp.wait()

    fill(slot)
    for cp in copies(slot):
        cp.start()

    @pl.when(step == n_steps - 1)
    def _():
        for s in ([slot, 1 - slot] if n_steps > 1 else [slot]):
            for cp in copies(s):
                cp.wait()


def _mix_bwd_call(dmixed, proj, gate_col0, b_gate, y_a, y_c):
    S, D = y_a.shape
    w = GATE_COLS
    nc = D // w
    nr = S // GATE_ROWS
    ga0, gc0 = gate_col0 // w, (gate_col0 + D) // w

    def body(dm_ref, ga_ref, gc_ref, ba_ref, bc_ref, ya_ref, yc_ref, dya_ref, dyc_ref, dproj_ref, dba_ref, dbc_ref,
             stage_ref, sems):
        j, i = pl.program_id(0), pl.program_id(1)
        dm = dm_ref[...]
        sa = _sigmoid(ga_ref[...] + ba_ref[...])
        sc = _sigmoid(gc_ref[...] + bc_ref[...])
        dya_ref[...] = (dm * sa).astype(BF16)
        dyc_ref[...] = (dm * sc).astype(BF16)
        dga = dm * ya_ref[...] * sa * (1.0 - sa)
        dgc = dm * yc_ref[...] * sc * (1.0 - sc)

        def fill(slot):
            stage_ref[slot, 0] = dga.astype(BF16)
            stage_ref[slot, 1] = dgc.astype(BF16)

        rows = pl.ds(pl.multiple_of(i * GATE_ROWS, GATE_ROWS), GATE_ROWS)
        _staged_window_stores(stage_ref, dproj_ref, sems, j * nr + i, nc * nr, rows,
                              [gate_col0 + j * w, gate_col0 + D + j * w], fill)
        pa = jnp.sum(dga, axis=0, keepdims=True)
        pc = jnp.sum(dgc, axis=0, keepdims=True)

        @pl.when(i == 0)
        def _():
            dba_ref[...] = pa
            dbc_ref[...] = pc

        @pl.when(i > 0)
        def _():
            dba_ref[...] += pa
            dbc_ref[...] += pc

    tile = lambda off: pl.BlockSpec((GATE_ROWS, w), lambda j, i: (i, off + j))
    vec = lambda off: pl.BlockSpec((1, w), lambda j, i: (0, off + j))
    return pl.pallas_call(
        body, name="gate_mix_bwd", grid=(nc, nr),
        in_specs=[tile(0), tile(ga0), tile(gc0), vec(0), vec(nc), tile(0), tile(0)],
        out_specs=[tile(0), tile(0), ANY, vec(0), vec(0)],
        out_shape=[jax.ShapeDtypeStruct((S, D), BF16)] * 2 + [jax.ShapeDtypeStruct((S, proj.shape[1]), BF16)] + [
                   jax.ShapeDtypeStruct((1, D), F32), jax.ShapeDtypeStruct((1, D), F32)],
        scratch_shapes=[pltpu.VMEM((2, 2, GATE_ROWS, w), BF16), pltpu.SemaphoreType.DMA((2, 2))],
        compiler_params=_params(("arbitrary", "arbitrary")),
    )(dmixed, proj, proj, b_gate, b_gate, y_a, y_c)


def _res1_call(x, out, g_post, g_pre):
    S, D = x.shape

    def body(x_ref, o_ref, gp_ref, gq_ref, x1_ref, h2_ref):
        x1 = x_ref[...] + _rms(o_ref[...], gp_ref[...])
        x1_ref[...] = x1
        h2_ref[...] = _rms(x1, gq_ref[...]).astype(BF16)

    return pl.pallas_call(
        body, name="residual_mix", grid=(S // ROW_TILE,),
        in_specs=[_row_spec(D), _row_spec(D), _vec_spec(D), _vec_spec(D)],
        out_specs=[_row_spec(D), _row_spec(D)],
        out_shape=[jax.ShapeDtypeStruct((S, D), F32), jax.ShapeDtypeStruct((S, D), BF16)],
        compiler_params=_params(("parallel",)),
    )(x, out, g_post, g_pre)


def _loss_call(y, x1, g_post, target):
    S, D = y.shape

    def body(y_ref, x1_ref, g_ref, t_ref, loss_ref, dx_ref, dy_ref, dg_ref):
        yv, gv = y_ref[...], g_ref[...]
        err = x1_ref[...] + _rms(yv, gv) - t_ref[...]
        dx2 = err * (1.0 / D)
        dx_ref[...] = dx2
        dy, dg = _rms_bwd(yv, gv, dx2)
        dy_ref[...] = dy.astype(BF16)
        _accumulate(dg_ref, dg)
        part = 0.5 * jnp.sum(jnp.mean(err * err, axis=-1, keepdims=True), axis=0, keepdims=True)
        _accumulate(loss_ref, jnp.broadcast_to(part, (SUBLANES, LANES)))

    return pl.pallas_call(
        body, name="residual_ffn_loss", grid=(S // ROW_TILE,),
        in_specs=[_row_spec(D), _row_spec(D), _vec_spec(D), _row_spec(D)],
        out_specs=[pl.BlockSpec((SUBLANES, LANES), lambda i: (0, 0)), _row_spec(D), _row_spec(D), _vec_spec(D)],
        out_shape=[jax.ShapeDtypeStruct((SUBLANES, LANES), F32), jax.ShapeDtypeStruct((S, D), F32),
                   jax.ShapeDtypeStruct((S, D), BF16), jax.ShapeDtypeStruct((1, D), F32)],
        compiler_params=_params(("arbitrary",)),
    )(y, x1, g_post, target)


def _mid_bwd_call(x1, g_pre, dh2, dx2, out, g_post):
    S, D = x1.shape

    def body(x1_ref, gq_ref, dh_ref, dx2_ref, o_ref, gp_ref, dx1_ref, do_ref, dgq_ref, dgp_ref):
        d, dgq = _rms_bwd(x1_ref[...], gq_ref[...], dh_ref[...])
        dx1 = dx2_ref[...] + d
        dx1_ref[...] = dx1
        do, dgp = _rms_bwd(o_ref[...], gp_ref[...], dx1)
        do_ref[...] = do.astype(BF16)
        _accumulate(dgq_ref, dgq)
        _accumulate(dgp_ref, dgp)

    return pl.pallas_call(
        body, name="residual_mix_bwd", grid=(S // ROW_TILE,),
        in_specs=[_row_spec(D), _vec_spec(D), _row_spec(D), _row_spec(D), _row_spec(D), _vec_spec(D)],
        out_specs=[_row_spec(D), _row_spec(D), _vec_spec(D), _vec_spec(D)],
        out_shape=[jax.ShapeDtypeStruct((S, D), F32), jax.ShapeDtypeStruct((S, D), BF16)] + [jax.ShapeDtypeStruct((1, D), F32)] * 2,
        compiler_params=_params(("arbitrary",)),
    )(x1, g_pre, dh2, dx2, out, g_post)


def _in_bwd_call(x, g, dh1, dx1):
    S, D = x.shape

    def body(x_ref, g_ref, dh_ref, dx1_ref, gx_ref, dg_ref):
        d, dg = _rms_bwd(x_ref[...], g_ref[...], dh_ref[...])
        gx_ref[...] = dx1_ref[...] + d
        _accumulate(dg_ref, dg)

    return pl.pallas_call(
        body, name="rms_mix_pre_bwd", grid=(S // ROW_TILE,),
        in_specs=[_row_spec(D), _vec_spec(D), _row_spec(D), _row_spec(D)],
        out_specs=[_row_spec(D), _vec_spec(D)],
        out_shape=[jax.ShapeDtypeStruct((S, D), F32), jax.ShapeDtypeStruct((1, D), F32)],
        compiler_params=_params(("arbitrary",)),
    )(x, g, dh1, dx1)


def _bucket_table(dilation):
    qi = np.arange(SPAN)[:, None]
    ki = np.arange(2 * SPAN)[None, :]
    dist = np.maximum(qi + SPAN - ki, 0) * dilation
    max_exact = N_BUCKETS // 2
    d = np.maximum(dist, 1).astype(np.float64)
    large = max_exact + (np.log(d / max_exact) / math.log(MAX_DISTANCE / max_exact) * (N_BUCKETS - max_exact)).astype(np.int32)
    large = np.minimum(large, N_BUCKETS - 1)
    return np.where(dist < max_exact, dist, large).astype(np.int32)


def _bucket_tables():
    return jnp.asarray(np.stack([_bucket_table(r) for _, r in DILATED_PATTERNS]))


def _bias_table_call(rel_bias, buckets):
    def body(rb_ref, bk_ref, o_ref):
        for h in range(N_HEADS):
            bk = bk_ref[h // HEADS_PER_GROUP]

            def step(b, acc):
                return jnp.where(bk == b, rb_ref[b, h], acc)

            o_ref[h] = lax.fori_loop(0, N_BUCKETS, step, jnp.zeros((SPAN, 2 * SPAN), F32))

    return pl.pallas_call(
        body, name="rel_bias_table",
        in_specs=[pl.BlockSpec(memory_space=pltpu.SMEM), pl.BlockSpec(memory_space=pltpu.VMEM)],
        out_specs=pl.BlockSpec(memory_space=pltpu.VMEM),
        out_shape=jax.ShapeDtypeStruct((N_HEADS, SPAN, 2 * SPAN), F32),
    )(rel_bias, buckets)


def _bias_grad_call(dbias, buckets):
    def body(db_ref, bk_ref, o_ref, rows_ref):
        for h in range(N_HEADS):
            bk = bk_ref[h // HEADS_PER_GROUP]
            dv = db_ref[h]

            def step(b, carry):
                rows_ref[h, b] = jnp.sum(jnp.where(bk == b, dv, 0.0), axis=0, keepdims=True)
                return carry

            lax.fori_loop(0, N_BUCKETS, step, 0)
        o_ref[...] = jnp.sum(rows_ref[...], axis=-1, keepdims=True)

    out = pl.pallas_call(
        body, name="rel_bias_grad",
        in_specs=[pl.BlockSpec(memory_space=pltpu.VMEM), pl.BlockSpec(memory_space=pltpu.VMEM)],
        out_specs=pl.BlockSpec(memory_space=pltpu.VMEM),
        out_shape=jax.ShapeDtypeStruct((N_HEADS, N_BUCKETS, 1, 1), F32),
        scratch_shapes=[pltpu.VMEM((N_HEADS, N_BUCKETS, 1, 2 * SPAN), F32)],
    )(dbias, buckets)
    return out.reshape(N_HEADS, N_BUCKETS).T


def _dot_nt(a, b):
    return lax.dot_general(a, b, (((1,), (1,)), ((), ())), preferred_element_type=F32)


def _dot_nn(a, b):
    return lax.dot_general(a, b, (((1,), (0,)), ((), ())), preferred_element_type=F32)


def _dot_tn(a, b):
    return lax.dot_general(a, b, (((0,), (0,)), ((), ())), preferred_element_type=F32)


def _band_masks(n, nb):
    qi = lax.broadcasted_iota(jnp.int32, (SPAN, SPAN), 0)
    ki = lax.broadcasted_iota(jnp.int32, (SPAN, SPAN), 1)
    prev_ok = jnp.logical_and(ki >= qi, n > 0)
    cur_ok = ki <= qi
    next_ok = jnp.logical_and(ki >= qi, n < nb - 1)
    return prev_ok, cur_ok, next_ok


def _wide_band_mask(n):
    qi = lax.broadcasted_iota(jnp.int32, (SPAN, 2 * SPAN), 0)
    ki = lax.broadcasted_iota(jnp.int32, (SPAN, 2 * SPAN), 1)
    prev_ok = jnp.logical_and(jnp.logical_and(ki < SPAN, ki >= qi), n > 0)
    cur_ok = jnp.logical_and(ki >= SPAN, ki - SPAN <= qi)
    return jnp.logical_or(prev_ok, cur_ok)


def _attn_plan(S, group):
    r = DILATED_PATTERNS[group][1]
    hp, per = (HEADS_PER_GROUP, 1) if r == 1 else (2, 4)
    return r, S // (r * SPAN), hp, per


def _residue_rows(rho, r):
    return slice(None) if r == 1 else pl.ds(rho, SPAN, stride=r)


def _for_residues(r, per, fn):
    if r == per:
        for u in range(per):
            fn(u)
        return

    def step(i, carry):
        for u in range(per):
            fn(i * per + u)
        return carry

    lax.fori_loop(0, r // per, step, 0)


def _attn_fwd_call(proj, bias, group):
    S = proj.shape[0]
    r, nb, hp, per = _attn_plan(S, group)
    scale = HEAD_DIM ** -0.5
    kinds = ("q", "kp", "kc", "vp", "vc") if nb > 1 else ("q", "kc", "vc")

    per_kind = _refs_per_kind(r, hp)

    def body(*refs):
        ins = {kind: refs[i * per_kind:(i + 1) * per_kind] for i, kind in enumerate(kinds)}
        b_ref, o_ref, lse_ref = refs[len(kinds) * per_kind:]
        n = pl.program_id(1)
        prev_ok, cur_ok, _ = _band_masks(n, nb)

        band_ok = _wide_band_mask(n) if nb > 1 else cur_ok

        def residue(rho):
            rows = _residue_rows(rho, r)
            for j in range(hp):
                get = lambda kind: _head_rows(ins[kind], j, rows, r).astype(BF16)
                q = get("q")
                if nb > 1:
                    keys, vals, bias_j = jnp.concatenate([get("kp"), get("kc")], axis=0), jnp.concatenate([get("vp"), get("vc")], axis=0), b_ref[j]
                else:
                    keys, vals, bias_j = get("kc"), get("vc"), b_ref[j, :, SPAN:]
                s = jnp.where(band_ok, _dot_nt(q, keys) * scale + bias_j, NEG_INF)
                m = jnp.max(s, axis=-1, keepdims=True)
                p = jnp.exp(s - m)
                den = jnp.sum(p, axis=-1, keepdims=True)
                o_ref[j, rows, :] = _dot_nn(p.astype(BF16), vals) / den
                lse_ref[j, rows, :] = jnp.broadcast_to(m + jnp.log(den), (SPAN, HEAD_DIM))

        _for_residues(r, per, residue)

    in_specs = [_head_spec(r, nb, hp, kind, group, jj) for kind in kinds for jj in range(per_kind)]
    in_specs.append(pl.BlockSpec((hp, SPAN, 2 * SPAN), lambda j, n: (group * (HEADS_PER_GROUP // hp) + j, 0, 0)))
    out = pl.BlockSpec((hp, r * SPAN, HEAD_DIM), lambda j, n: (j, n, 0))
    return pl.pallas_call(
        body, name=f"attn_fwd_g{group}", grid=(HEADS_PER_GROUP // hp, nb),
        in_specs=in_specs, out_specs=[out] * 2,
        out_shape=[jax.ShapeDtypeStruct((HEADS_PER_GROUP, S, HEAD_DIM), F32)] * 2,
        compiler_params=_params(("parallel", "parallel"), VMEM_LIMIT),
    )(*([proj] * (len(in_specs) - 1)), bias)


_PROJ_PART = dict(q=0, qn=0, kp=1, kc=1, vp=2, vc=2)


def _refs_per_kind(r, hp):
    return 1 if r == 1 else hp


def _head_rows(refs, j, rows, r):
    return refs[0][:, j * HEAD_DIM:(j + 1) * HEAD_DIM] if r == 1 else refs[j][rows, :]


def _head_spec(r, nb, hp, kind, group, jj):
    if kind in _PROJ_PART:
        base = (_PROJ_PART[kind] * N_GROUPS + group) * HEADS_PER_GROUP
    else:
        base = 0
    if kind.endswith("p"):
        row = lambda n: jnp.maximum(n - 1, 0)
    elif kind.endswith("n"):
        row = lambda n: jnp.minimum(n + 1, nb - 1)
    else:
        row = lambda n: n
    if r == 1:
        return pl.BlockSpec((SPAN, hp * HEAD_DIM), lambda j, n: (row(n), base // hp + j))
    return pl.BlockSpec((r * SPAN, HEAD_DIM), lambda j, n: (row(n), base + j * hp + jj))


def _attn_merge_call(parts):
    S = parts[0].shape[1]

    def body(o1, s1, o2, s2, o3, s3, a_ref, ab_ref, lse_ref):
        for j in range(HEADS_PER_GROUP):
            sl = slice(j * HEAD_DIM, (j + 1) * HEAD_DIM)
            mx = jnp.maximum(jnp.maximum(s1[j], s2[j]), s3[j])
            w1 = jnp.exp(s1[j] - mx)
            w2 = jnp.exp(s2[j] - mx)
            w3 = jnp.exp(s3[j] - mx)
            den = w1 + w2 + w3
            a = (w1 * o1[j] + w2 * o2[j] + w3 * o3[j]) / den
            a_ref[:, sl] = a
            ab_ref[:, sl] = a.astype(BF16)
            lse_ref[:, sl] = mx + jnp.log(den)

    heads = pl.BlockSpec((HEADS_PER_GROUP, ROW_TILE, HEAD_DIM), lambda i: (0, i, 0))
    return pl.pallas_call(
        body, name="attn_merge", grid=(S // ROW_TILE,),
        in_specs=[heads] * 6, out_specs=[_row_spec(GROUP_WIDTH)] * 3,
        out_shape=[jax.ShapeDtypeStruct((S, GROUP_WIDTH), F32), jax.ShapeDtypeStruct((S, GROUP_WIDTH), BF16),
                   jax.ShapeDtypeStruct((S, GROUP_WIDTH), F32)],
        compiler_params=_params(("parallel",)),
    )(*parts)


def _attn_delta_call(a, da):
    S = a.shape[0]

    def body(a_ref, da_ref, d_ref):
        for j in range(HEADS_PER_GROUP):
            sl = slice(j * HEAD_DIM, (j + 1) * HEAD_DIM)
            d = jnp.sum(a_ref[:, sl] * da_ref[:, sl], axis=-1, keepdims=True)
            d_ref[:, sl] = jnp.broadcast_to(d, (ROW_TILE, HEAD_DIM))

    return pl.pallas_call(
        body, name="attn_delta", grid=(S // ROW_TILE,),
        in_specs=[_row_spec(GROUP_WIDTH)] * 2, out_specs=_row_spec(GROUP_WIDTH),
        out_shape=jax.ShapeDtypeStruct((S, GROUP_WIDTH), F32),
        compiler_params=_params(("parallel",)),
    )(a, da)


def _attn_bwd_call(proj, bias, da, lse, delta, group, dproj):
    S = proj.shape[0]
    r, nb, hp, per = _attn_plan(S, group)
    scale = HEAD_DIM ** -0.5
    kinds = ("q", "qn", "kp", "kc", "vp", "vc", "da", "dan", "lse", "lsen", "dl", "dln") if nb > 1 else ("q", "kc", "vc", "da", "lse", "dl")
    source = dict(da=da, dan=da, lse=lse, lsen=lse, dl=delta, dln=delta)

    per_kind = _refs_per_kind(r, hp)
    per_group = HEADS_PER_GROUP // hp
    block_rows = r * SPAN

    def body(*refs):
        ins = {kind: refs[i * per_kind:(i + 1) * per_kind] for i, kind in enumerate(kinds)}
        b_ref, _, dproj_ref, db_ref, stage_ref, sems = refs[len(kinds) * per_kind:][:6]
        strided_ref = None if r == 1 else refs[-1]
        jg, n = pl.program_id(0), pl.program_id(1)
        prev_ok, cur_ok, next_ok = _band_masks(n, nb)

        @pl.when(n == 0)
        def _():
            db_ref[...] = jnp.zeros_like(db_ref)

        band_ok = _wide_band_mask(n) if nb > 1 else cur_ok

        def fill(slot):
            def put(part, j, rows, value):
                if r == 1:
                    stage_ref[slot, part, :, j * HEAD_DIM:(j + 1) * HEAD_DIM] = value.astype(BF16)
                else:
                    strided_ref[part, j, rows, :] = value

            _for_residues(r, per, functools.partial(residue, put))
            if r > 1:
                for part in range(3):
                    for j in range(hp):
                        for t0 in range(0, block_rows, ROW_TILE):
                            stage_ref[slot, part, t0:t0 + ROW_TILE, j * HEAD_DIM:(j + 1) * HEAD_DIM] = (
                                strided_ref[part, j, t0:t0 + ROW_TILE, :].astype(BF16))

        def residue(put, rho):
            rows = _residue_rows(rho, r)
            for j in range(hp):
                get = lambda kind: _head_rows(ins[kind], j, rows, r)
                q = get("q").astype(BF16)
                kc = get("kc").astype(BF16)
                vc = get("vc").astype(BF16)
                dav = get("da").astype(BF16)
                lse_q, dl_q = get("lse"), get("dl")
                if nb == 1:
                    pc = jnp.exp(jnp.where(cur_ok, _dot_nt(q, kc) * scale + b_ref[j, :, SPAN:], NEG_INF) - lse_q)
                    dsc = pc * (_dot_nt(dav, vc) - dl_q)
                    dsc_b = dsc.astype(BF16)
                    dq = _dot_nn(dsc_b, kc)
                    dk = _dot_tn(dsc_b, q)
                    dv = _dot_tn(pc.astype(BF16), dav)
                    db_ref[j, :, SPAN:] += dsc
                else:
                    qn = get("qn").astype(BF16)
                    dan = get("dan").astype(BF16)
                    keys = jnp.concatenate([get("kp").astype(BF16), kc], axis=0)
                    vals = jnp.concatenate([get("vp").astype(BF16), vc], axis=0)
                    wide = lambda t: jnp.concatenate([t, t], axis=1)
                    p = jnp.exp(jnp.where(band_ok, _dot_nt(q, keys) * scale + b_ref[j], NEG_INF) - wide(lse_q))
                    ds = p * (_dot_nt(dav, vals) - wide(dl_q))
                    dq = _dot_nn(ds.astype(BF16), keys)
                    db_ref[j] += ds
                    pn = jnp.exp(jnp.where(next_ok, _dot_nt(qn, kc) * scale + b_ref[j, :, :SPAN], NEG_INF) - get("lsen"))
                    dsn = pn * (_dot_nt(dan, vc) - get("dln"))
                    both = lambda cur_part, next_part: jnp.concatenate([cur_part.astype(BF16), next_part.astype(BF16)], axis=0)
                    dk = _dot_tn(both(ds[:, SPAN:], dsn), jnp.concatenate([q, qn], axis=0))
                    dv = _dot_tn(both(p[:, SPAN:], pn), jnp.concatenate([dav, dan], axis=0))
                put(0, j, rows, dq * scale)
                put(1, j, rows, dk * scale)
                put(2, j, rows, dv)

        cols = [(part * N_GROUPS + group) * GROUP_WIDTH + jg * (hp * HEAD_DIM) for part in range(3)]
        rows = pl.ds(pl.multiple_of(n * block_rows, SPAN), block_rows)
        _staged_window_stores(stage_ref, dproj_ref, sems, jg * nb + n, per_group * nb, rows, cols, fill)

    band = (hp, SPAN, 2 * SPAN)
    in_specs = [_head_spec(r, nb, hp, kind, group, jj) for kind in kinds for jj in range(per_kind)]
    in_specs += [pl.BlockSpec(band, lambda j, n: (group * per_group + j, 0, 0)), ANY]
    operands = [source.get(kind, proj) for kind in kinds for _ in range(per_kind)] + [bias, dproj]
    scratch = [pltpu.VMEM((2, 3, block_rows, hp * HEAD_DIM), BF16), pltpu.SemaphoreType.DMA((2, 3))]
    if r > 1:
        scratch.append(pltpu.VMEM((3, hp, block_rows, HEAD_DIM), F32))
    return pl.pallas_call(
        body, name=f"attn_bwd_g{group}", grid=(per_group, nb),
        in_specs=in_specs,
        out_specs=[ANY, pl.BlockSpec(band, lambda j, n: (j, 0, 0))],
        out_shape=[jax.ShapeDtypeStruct(dproj.shape, BF16), jax.ShapeDtypeStruct((HEADS_PER_GROUP, SPAN, 2 * SPAN), F32)],
        input_output_aliases={len(operands) - 1: 0},
        scratch_shapes=scratch,
        compiler_params=_params(("arbitrary", "arbitrary"), VMEM_LIMIT),
    )(*operands)


def _tap_rows(xpad_ref, t0, k, width, pad):
    return xpad_ref[pl.ds(t0 + (pad - (width - 1 - k)), TIME_BLOCK), :]


def _conv_block(xpad_ref, t0, w_ref, width, pad):
    acc = None
    for k in range(width):
        term = w_ref[k:k + 1, :] * _tap_rows(xpad_ref, t0, k, width, pad)
        acc = term if acc is None else acc + term
    return acc


def _conv_transpose_block(dpad_ref, t0, w_ref, width):
    acc = None
    for k in range(width):
        term = w_ref[k:k + 1, :] * dpad_ref[pl.ds(t0 + (width - 1 - k), TIME_BLOCK), :]
        acc = term if acc is None else acc + term
    return acc


def _conv_weight_grad(xpad_ref, t0, dy, dw_ref, width, pad):
    for k in range(width):
        dw_ref[k:k + 1, :] += jnp.sum(dy * _tap_rows(xpad_ref, t0, k, width, pad), axis=0, keepdims=True)


def _time_loop(S, step, skip_first=0, skip_last=0):
    def it(tb, carry):
        step(pl.multiple_of(tb * TIME_BLOCK, TIME_BLOCK))
        return carry

    lax.fori_loop(skip_first, S // TIME_BLOCK - skip_last, it, 0)


def _fill_head(head_ref, x_ref, pad):
    head_ref[0:pad, :] = jnp.zeros((pad, LANES), F32)
    head_ref[pad:, :] = x_ref[0:TIME_BLOCK, :]


def _fill_tail(tail_ref, x_ref, pad):
    S = x_ref.shape[0]
    tail_ref[0:TIME_BLOCK, :] = x_ref[S - TIME_BLOCK:S, :]
    tail_ref[TIME_BLOCK:, :] = jnp.zeros((pad, LANES), F32)


def _conv_fwd_call(proj, col0, w, b):
    S = proj.shape[0]
    C = w.shape[1]
    nt = C // LANES
    v0, g0 = col0 // LANES, (col0 + C) // LANES

    def body(val_ref, gate_ref, w_ref, b_ref, o_ref, pad_ref):
        pad_ref[0:CONV_PAD, :] = jnp.zeros((CONV_PAD, LANES), F32)
        pad_ref[CONV_PAD:, :] = val_ref[...] * _sigmoid(gate_ref[...])

        def step(t0):
            o_ref[pl.ds(t0, TIME_BLOCK), :] = _conv_block(pad_ref, t0, w_ref, CONV_WIDTH, CONV_PAD) + b_ref[...]

        _time_loop(S, step)

    seq = lambda off: pl.BlockSpec((S, LANES), lambda i: (0, off + i))
    return pl.pallas_call(
        body, name="conv_module", grid=(nt,),
        in_specs=[seq(v0), seq(g0), pl.BlockSpec((CONV_WIDTH, LANES), lambda i: (0, i)), pl.BlockSpec((1, LANES), lambda i: (0, i))],
        out_specs=seq(0), out_shape=jax.ShapeDtypeStruct((S, C), F32),
        scratch_shapes=[pltpu.VMEM((S + CONV_PAD, LANES), F32)],
        compiler_params=_params(("parallel",)),
    )(proj, proj, w, b)


def _conv_bwd_call(proj, col0, w, dc1, dproj):
    S = proj.shape[0]
    C = w.shape[1]
    nt = C // LANES
    v0, g0 = col0 // LANES, (col0 + C) // LANES

    def body(val_ref, gate_ref, w_ref, dy_ref, _, dproj_ref, dw_ref, db_ref, xpad_ref, tail_ref, dwacc_ref, stage_ref, sems):
        i = pl.program_id(0)
        xpad_ref[0:CONV_PAD, :] = jnp.zeros((CONV_PAD, LANES), F32)
        xpad_ref[CONV_PAD:, :] = val_ref[...] * _sigmoid(gate_ref[...])
        _fill_tail(tail_ref, dy_ref, CONV_PAD)
        dwacc_ref[...] = jnp.zeros_like(dwacc_ref)

        def fill(slot):
            def block(t0, dy_src, dy_t0):
                rows = pl.ds(t0, TIME_BLOCK)
                _conv_weight_grad(xpad_ref, t0, dy_ref[rows, :], dwacc_ref, CONV_WIDTH, CONV_PAD)
                dc0 = _conv_transpose_block(dy_src, dy_t0, w_ref, CONV_WIDTH)
                sg = _sigmoid(gate_ref[rows, :])
                stage_ref[slot, 0, rows, :] = (dc0 * sg).astype(BF16)
                stage_ref[slot, 1, rows, :] = (dc0 * val_ref[rows, :] * sg * (1.0 - sg)).astype(BF16)

            _time_loop(S, lambda t0: block(t0, dy_ref, t0), skip_last=1)
            block(S - TIME_BLOCK, tail_ref, 0)

        _staged_window_stores(stage_ref, dproj_ref, sems, i, nt, pl.ds(0, S),
                              [col0 + i * LANES, col0 + C + i * LANES], fill)
        dw_ref[...] = dwacc_ref[...]
        db_ref[...] = jnp.sum(dy_ref[...], axis=0, keepdims=True)

    seq = lambda off: pl.BlockSpec((S, LANES), lambda i: (0, off + i))
    return pl.pallas_call(
        body, name="conv_module_bwd", grid=(nt,),
        in_specs=[seq(v0), seq(g0), pl.BlockSpec((CONV_WIDTH, LANES), lambda i: (0, i)), seq(0), ANY],
        out_specs=[ANY, pl.BlockSpec((CONV_PAD, LANES), lambda i: (0, i)), pl.BlockSpec((1, LANES), lambda i: (0, i))],
        out_shape=[jax.ShapeDtypeStruct(dproj.shape, BF16),
                   jax.ShapeDtypeStruct((CONV_PAD, C), F32), jax.ShapeDtypeStruct((1, C), F32)],
        input_output_aliases={4: 0},
        scratch_shapes=[pltpu.VMEM((S + CONV_PAD, LANES), F32), pltpu.VMEM((TIME_BLOCK + CONV_PAD, LANES), F32),
                        pltpu.VMEM((CONV_PAD, LANES), F32),
                        pltpu.VMEM((2, 2, S, LANES), BF16), pltpu.SemaphoreType.DMA((2, 2))],
        compiler_params=_params(("arbitrary",)),
    )(proj, proj, w, dc1, dproj)


def _ffn_fwd_call(u, w, b):
    S, C2 = u.shape
    C = C2 // 2
    nt = C // LANES

    def body(ug_ref, uv_ref, wg_ref, wv_ref, bg_ref, bv_ref, f_ref, hg_ref, hv_ref):
        _fill_head(hg_ref, ug_ref, FFN_PAD)
        _fill_head(hv_ref, uv_ref, FFN_PAD)

        def block(t0, xg_ref, xv_ref, x_t0, pad):
            cg = _conv_block(xg_ref, x_t0, wg_ref, FFN_CONV_WIDTH, pad) + bg_ref[...]
            cv = _conv_block(xv_ref, x_t0, wv_ref, FFN_CONV_WIDTH, pad) + bv_ref[...]
            f_ref[pl.ds(t0, TIME_BLOCK), :] = (_gelu(cg) * cv).astype(BF16)

        block(0, hg_ref, hv_ref, 0, FFN_PAD)
        _time_loop(S, lambda t0: block(t0, ug_ref, uv_ref, t0, 0), skip_first=1)

    seq = lambda off: pl.BlockSpec((S, LANES), lambda i: (0, off + i))
    wsp = lambda off: pl.BlockSpec((FFN_CONV_WIDTH, LANES), lambda i: (0, off + i))
    bsp = lambda off: pl.BlockSpec((1, LANES), lambda i: (0, off + i))
    return pl.pallas_call(
        body, name="ffn_conv_geglu", grid=(nt,),
        in_specs=[seq(0), seq(nt), wsp(0), wsp(nt), bsp(0), bsp(nt)],
        out_specs=seq(0), out_shape=jax.ShapeDtypeStruct((S, C), BF16),
        scratch_shapes=[pltpu.VMEM((FFN_PAD + TIME_BLOCK, LANES), F32)] * 2,
        compiler_params=_params(("parallel",)),
    )(u, u, w, w, b, b)


def _ffn_bwd_call(u, w, b, df):
    S, C2 = u.shape
    C = C2 // 2
    nt = C // LANES

    def body(ug_ref, uv_ref, wg_ref, wv_ref, bg_ref, bv_ref, df_ref,
             du_ref, dwg_ref, dwv_ref, dbg_ref, dbv_ref,
             hg_ref, hv_ref, dg_ref, dv_ref, dwg_acc, dwv_acc, dbg_acc, dbv_acc):
        zeros = jnp.zeros((FFN_PAD, LANES), F32)
        _fill_head(hg_ref, ug_ref, FFN_PAD)
        _fill_head(hv_ref, uv_ref, FFN_PAD)
        dg_ref[S:, :] = zeros
        dv_ref[S:, :] = zeros
        dwg_acc[...] = jnp.zeros_like(dwg_acc)
        dwv_acc[...] = jnp.zeros_like(dwv_acc)
        dbg_acc[...] = jnp.zeros_like(dbg_acc)
        dbv_acc[...] = jnp.zeros_like(dbv_acc)

        def first(t0, xg_ref, xv_ref, x_t0, pad):
            rows = pl.ds(t0, TIME_BLOCK)
            cg = _conv_block(xg_ref, x_t0, wg_ref, FFN_CONV_WIDTH, pad) + bg_ref[...]
            cv = _conv_block(xv_ref, x_t0, wv_ref, FFN_CONV_WIDTH, pad) + bv_ref[...]
            dfb = df_ref[rows, :]
            gelu, gelu_grad = _gelu_and_grad(cg)
            dcg = dfb * cv * gelu_grad
            dcv = dfb * gelu
            dg_ref[rows, :] = dcg
            dv_ref[rows, :] = dcv
            _conv_weight_grad(xg_ref, x_t0, dcg, dwg_acc, FFN_CONV_WIDTH, pad)
            _conv_weight_grad(xv_ref, x_t0, dcv, dwv_acc, FFN_CONV_WIDTH, pad)
            dbg_acc[...] += jnp.sum(dcg, axis=0, keepdims=True)
            dbv_acc[...] += jnp.sum(dcv, axis=0, keepdims=True)

        def second(t0):
            rows = pl.ds(t0, TIME_BLOCK)
            du_ref[0, rows, :] = _conv_transpose_block(dg_ref, t0, wg_ref, FFN_CONV_WIDTH).astype(BF16)
            du_ref[1, rows, :] = _conv_transpose_block(dv_ref, t0, wv_ref, FFN_CONV_WIDTH).astype(BF16)

        first(0, hg_ref, hv_ref, 0, FFN_PAD)
        _time_loop(S, lambda t0: first(t0, ug_ref, uv_ref, t0, 0), skip_first=1)
        _time_loop(S, second)
        dwg_ref[...] = dwg_acc[...]
        dwv_ref[...] = dwv_acc[...]
        dbg_ref[...] = dbg_acc[...]
        dbv_ref[...] = dbv_acc[...]

    seq = lambda off: pl.BlockSpec((S, LANES), lambda i: (0, off + i))
    wsp = lambda off: pl.BlockSpec((FFN_CONV_WIDTH, LANES), lambda i: (0, off + i))
    bsp = lambda off: pl.BlockSpec((1, LANES), lambda i: (0, off + i))
    return pl.pallas_call(
        body, name="ffn_conv_geglu_bwd", grid=(nt,),
        in_specs=[seq(0), seq(nt), wsp(0), wsp(nt), bsp(0), bsp(nt), seq(0)],
        out_specs=[pl.BlockSpec((2, S, LANES), lambda i: (0, 0, i)),
                   pl.BlockSpec((SUBLANES, LANES), lambda i: (0, i)), pl.BlockSpec((SUBLANES, LANES), lambda i: (0, i)),
                   bsp(0), bsp(0)],
        out_shape=[jax.ShapeDtypeStruct((2, S, C), BF16)] + [jax.ShapeDtypeStruct((SUBLANES, C), F32)] * 2
        + [jax.ShapeDtypeStruct((1, C), F32)] * 2,
        scratch_shapes=[pltpu.VMEM((FFN_PAD + TIME_BLOCK, LANES), F32)] * 2 + [pltpu.VMEM((S + FFN_PAD, LANES), F32)] * 2
        + [pltpu.VMEM((SUBLANES, LANES), F32)] * 2
        + [pltpu.VMEM((1, LANES), F32)] * 2,
        compiler_params=_params(("parallel",)),
    )(u, u, w, w, b, b, df)


def _adamw(w_ref, g_ref, m_ref, v_ref, d_ref, mo_ref, vo_ref):
    gv = g_ref[...]
    mn = ADAM_B1 * m_ref[...] + (1.0 - ADAM_B1) * gv
    vn = ADAM_B2 * v_ref[...] + (1.0 - ADAM_B2) * (gv * gv)
    mo_ref[...] = mn
    vo_ref[...] = vn
    m_hat = mn * (1.0 / (1.0 - ADAM_B1 ** ADAM_STEP))
    v_hat = vn * (1.0 / (1.0 - ADAM_B2 ** ADAM_STEP))
    d_ref[...] = -ADAM_LR * (m_hat / (jnp.sqrt(v_hat) + ADAM_EPS) + ADAM_WD * w_ref[...])


def _adamw_call(w, g, m, v, name):
    R, C = w.shape
    tr = _row_tile(R, C)

    def body(w_ref, g_ref, m_ref, v_ref, go_ref, d_ref, mo_ref, vo_ref):
        go_ref[...] = g_ref[...]
        _adamw(w_ref, g_ref, m_ref, v_ref, d_ref, mo_ref, vo_ref)

    spec = pl.BlockSpec((tr, C), lambda i: (i, 0))
    return pl.pallas_call(
        body, name=name, grid=(R // tr,),
        in_specs=[spec] * 4, out_specs=[spec] * 4,
        out_shape=[jax.ShapeDtypeStruct((R, C), F32)] * 4,
        compiler_params=_params(("parallel",)),
    )(w, g, m, v)


def _adamw_small_call(ws, gs, ms, vs):
    n = len(ws)

    def body(*refs):
        w_refs, g_refs, m_refs, v_refs, d_refs, mo_refs, vo_refs = (refs[i * n:(i + 1) * n] for i in range(7))
        for i in range(n):
            _adamw(w_refs[i], g_refs[i], m_refs[i], v_refs[i], d_refs[i], mo_refs[i], vo_refs[i])

    whole = pl.BlockSpec(memory_space=pltpu.VMEM)
    outs = pl.pallas_call(
        body, name="adamw_small",
        in_specs=[whole] * (4 * n), out_specs=[whole] * (3 * n),
        out_shape=[jax.ShapeDtypeStruct(w.shape, F32) for w in ws] * 3,
    )(*ws, *gs, *ms, *vs)
    return outs[:n], outs[n:2 * n], outs[2 * n:]


def _position():
    return lax.axis_index("x"), lax.axis_index("y"), lax.axis_index("c")


def _chip_peers(x, y):
    return [(x, 1 - y), (1 - x, y), (1 - x, 1 - y)]


def _half_rows(ref, core, rows):
    h = rows // 2
    start = pl.multiple_of(core * h, PACKED_ROWS)
    return ref.at[pl.ds(start, h), :] if len(ref.shape) == 2 else ref.at[:, pl.ds(start, h), :]


def _shard_half(ref, shard, core, rows):
    h = rows // 2
    return ref.at[shard, pl.ds(pl.multiple_of(core * h, PACKED_ROWS), h), :]


ANY = pl.BlockSpec(memory_space=pl.ANY)


def _first_hop_copies(srcs, lands):
    x, y, c = _position()
    chip = 2 * x + y
    targets = [(px, py, c) for px, py in _chip_peers(x, y)] + [(x, y, 1 - c)]
    rows = srcs[0].shape[0]
    out = []
    for i, (s, l) in enumerate(zip(srcs, lands)):
        for k, dev in enumerate(targets):
            if i == 0 and k < 3:
                out.append((_half_rows(s, c, rows), _shard_half(l, chip, c, rows), dev, k))
            else:
                out.append((s, l.at[chip], dev, len(targets) * i + k))
    return out


def _second_hop_copies(srcs, lands):
    x, y, c = _position()
    rows = lands[0].shape[1]
    out = []
    for k, (px, py) in enumerate(_chip_peers(x, y)):
        half = _shard_half(lands[0], 2 * px + py, c, rows)
        out.append((half, half, (x, y, 1 - c), k))
    return out


HBM_SPEC = pl.BlockSpec(memory_space=pltpu.HBM)
SEM_SPEC = pl.BlockSpec(memory_space=pltpu.SEMAPHORE)
DATAFLOW = pltpu.SideEffectType.DATAFLOW_SIDE_EFFECTING


def _in_hbm(a):
    return pltpu.with_memory_space_constraint(a, pltpu.HBM)


SIBLING_HANDSHAKE_IDS = dict(grad_exchange_start_w_up=1, grad_assemble_start_others=2, grad_assemble_start_w_in=3)


def _split_start(name, groups, after, carry=None, sibling_only=False):
    spans, arrays = [], []
    for srcs, lands, _, _ in groups:
        spans.append((len(arrays), len(srcs), len(lands)))
        arrays += list(srcs) + list(lands)
    if carry is not None:
        arrays.append(carry)
    na, ng = len(arrays), len(groups)

    def body(*refs):
        sems, token = refs[na + 1:na + 1 + 2 * ng], refs[-1]
        if sibling_only:
            x, y, c = _position()
            barrier = pltpu.get_barrier_semaphore()
            pl.semaphore_signal(barrier, inc=1, device_id=(x, y, 1 - c), device_id_type=MESH)
            pl.semaphore_wait(barrier, 1)
        for g, (_, _, _, copies) in enumerate(groups):
            off, ns, nl = spans[g]
            for src, dst, dev, idx in copies(refs[off:off + ns], refs[off + ns:off + ns + nl]):
                pltpu.make_async_remote_copy(src_ref=src, dst_ref=dst, send_sem=sems[2 * g].at[idx], recv_sem=sems[2 * g + 1].at[idx],
                                             device_id=dev, device_id_type=MESH).start()
        token[...] = jnp.zeros_like(token)

    outs = pl.pallas_call(
        body, name=name,
        in_specs=[HBM_SPEC] * na + [ANY],
        out_specs=[SEM_SPEC] * (2 * ng) + [HBM_SPEC] * na + [pl.BlockSpec(memory_space=pltpu.VMEM)],
        out_shape=[pltpu.SemaphoreType.DMA((n_sems,)) for _, _, n_sems, _ in groups for _ in range(2)]
        + [pltpu.HBM(a.shape, a.dtype) for a in arrays] + [jax.ShapeDtypeStruct((SUBLANES, LANES), F32)],
        input_output_aliases={i: 2 * ng + i for i in range(na)},
        compiler_params=pltpu.CompilerParams(has_side_effects=DATAFLOW,
                                             collective_id=SIBLING_HANDSHAKE_IDS[name] if sibling_only else None),
    )(*[_in_hbm(a) for a in arrays], after)
    started = []
    for g, (off, ns, nl) in enumerate(spans):
        thru = outs[2 * ng + off:2 * ng + off + ns + nl]
        started.append(dict(send=outs[2 * g], recv=outs[2 * g + 1], srcs=list(thru[:ns]), lands=list(thru[ns:]),
                            tile=outs[-1], token=outs[-1][0, 0], carry=None if carry is None else outs[2 * ng + na - 1]))
    return started


def _split_wait(name, started, copies, after):
    n, m = len(started["srcs"]), len(started["lands"])
    after = list(after) if isinstance(after, (list, tuple)) else [after]

    def body(*refs):
        src_refs, land_refs = refs[:n], refs[n:n + m]
        send_sem, recv_sem = refs[n + m], refs[n + m + 1]
        for src, dst, dev, idx in copies(src_refs, land_refs):
            cp = pltpu.make_async_remote_copy(src_ref=src, dst_ref=dst, send_sem=send_sem.at[idx], recv_sem=recv_sem.at[idx],
                                              device_id=dev, device_id_type=MESH)
            cp.wait_send()
            cp.wait_recv()

    arrays = started["srcs"] + started["lands"]
    outs = pl.pallas_call(
        body, name=name,
        in_specs=[HBM_SPEC] * (n + m) + [SEM_SPEC, SEM_SPEC] + [ANY] * len(after),
        out_specs=[HBM_SPEC] * (n + m),
        out_shape=[pltpu.HBM(a.shape, a.dtype) for a in arrays],
        input_output_aliases={i: i for i in range(n + m)},
        compiler_params=pltpu.CompilerParams(has_side_effects=DATAFLOW),
    )(*arrays, started["send"], started["recv"], *after)
    return list(outs)


def _gather_copies(srcs, lands):
    x, y, c = _position()
    chip = 2 * x + y
    targets = [(px, py, c) for px, py in _chip_peers(x, y)] + [(x, y, 1 - c)]
    return [(s, l.at[chip], dev, len(targets) * i + k) for i, (s, l) in enumerate(zip(srcs, lands)) for k, dev in enumerate(targets)]


def _sibling_copies(srcs, lands):
    x, y, c = _position()
    return [(_half_rows(srcs[0], 1 - c, srcs[0].shape[1]), lands[0], (x, y, 1 - c), 0)]


def _sibling_whole_copies(srcs, lands):
    x, y, c = _position()
    return [(srcs[0], lands[0], (x, y, 1 - c), 0)]


def _exchange_copies(srcs, lands):
    x, y, c = _position()
    return [(srcs[0].at[2 * px + py], lands[0].at[k], (px, py, c), k) for k, (px, py) in enumerate(_chip_peers(x, y))]


def _pair_sum_call(grad, recv, chip_core, name):
    _, h, B = recv.shape
    tr = _row_tile(h, B)

    def body(cc_ref, g_ref, r_ref, o_ref, ob_ref):
        s = g_ref[...] + r_ref[...]
        ob_ref[...] = s.astype(BF16)

        @pl.when(pl.program_id(1) == cc_ref[0])
        def _():
            o_ref[...] = s

    g_spec = pl.BlockSpec((None, tr, B), lambda i, q, cc_ref: (q, cc_ref[1] * (h // tr) + i, 0))
    spec = pl.BlockSpec((None, tr, B), lambda i, q, cc_ref: (q, i, 0))
    own_spec = pl.BlockSpec((tr, B), lambda i, q, cc_ref: (i, 0))
    return pl.pallas_call(
        body, name=name,
        grid_spec=pltpu.PrefetchScalarGridSpec(num_scalar_prefetch=1, grid=(h // tr, N_CHIPS), in_specs=[g_spec, spec],
                                               out_specs=[own_spec, spec]),
        out_shape=[jax.ShapeDtypeStruct((h, B), F32), jax.ShapeDtypeStruct(recv.shape, BF16)],
        compiler_params=_params(("parallel", "arbitrary")),
    )(chip_core, grad, recv)


def _chip_sum_call(partial, recv, chip_core, name):
    _, h, B = recv.shape
    tr = _row_tile(h, B)

    def body(cc_ref, p_ref, r_ref, o_ref):
        o_ref[...] = ((p_ref[...] + r_ref[0].astype(F32)) + r_ref[1].astype(F32)) + r_ref[2].astype(F32)

    return pl.pallas_call(
        body, name=name,
        grid_spec=pltpu.PrefetchScalarGridSpec(
            num_scalar_prefetch=1, grid=(h // tr,),
            in_specs=[pl.BlockSpec((tr, B), lambda i, cc_ref: (i, 0)),
                      pl.BlockSpec((3, tr, B), lambda i, cc_ref: (0, i, 0))],
            out_specs=pl.BlockSpec((tr, B), lambda i, cc_ref: (cc_ref[1] * (h // tr) + i, 0))),
        out_shape=jax.ShapeDtypeStruct((2 * h, B), F32),
        compiler_params=_params(("parallel",)),
    )(chip_core, partial, recv)


def _assemble_copies(srcs, lands):
    x, y, c = _position()
    out = []
    for i, land in enumerate(lands):
        half = _half_rows(land, c, land.shape[0])
        out.append((half, half, (x, y, 1 - c), i))
    return out


N_DEVICES = 8


def _allsum_copies(srcs, lands):
    x, y, c = _position()
    me = 4 * x + 2 * y + c
    out = []
    for k in range(1, N_DEVICES):
        peer = (1 - x if k & 4 else x, 1 - y if k & 2 else y, 1 - c if k & 1 else c)
        out.append((srcs[0], lands[0].at[me], peer, k - 1))
    return out


def _ordered_sum_call(mine, landed, me_chip, shapes, sharded_cols):
    rows = mine.shape[0]
    outs = [(s[0], n) if n else s for s, n in zip(shapes, sharded_cols)]

    def body(mc_ref, x_ref, l_ref, *refs):
        acc_ref = refs[-1]
        acc = jnp.where(mc_ref[0] == 0, x_ref[...], l_ref[0])
        for d in range(1, N_DEVICES):
            acc = acc + jnp.where(mc_ref[0] == d, x_ref[...], l_ref[d])
        acc_ref[...] = acc
        first = 0
        for o_ref, (r, c), n in zip(refs[:-1], shapes, sharded_cols):
            per_row = c // LANES

            def unpack(chip, o_ref=o_ref, r=r, n=n, per_row=per_row, first=first):
                for i in range(r):
                    for j in range((n or per_row * LANES) // LANES):
                        src = first + i * per_row + chip * ((n or 0) // LANES) + j
                        o_ref[i:i + 1, j * LANES:(j + 1) * LANES] = acc_ref[src:src + 1, :]

            if n:
                for q in range(N_CHIPS):
                    pl.when(mc_ref[1] == q)(functools.partial(unpack, q))
            else:
                unpack(0)
            first += r * per_row

    results = pl.pallas_call(
        body, name="small_grad_sum",
        in_specs=[pl.BlockSpec(memory_space=pltpu.SMEM), pl.BlockSpec(memory_space=pltpu.VMEM), pl.BlockSpec(memory_space=pltpu.VMEM)],
        out_specs=[pl.BlockSpec(memory_space=pltpu.VMEM)] * len(outs),
        out_shape=[jax.ShapeDtypeStruct(s, F32) for s in outs],
        scratch_shapes=[pltpu.VMEM((rows, LANES), F32)],
    )(me_chip, mine, landed)
    return results


def _pack(arrays):
    flat = jnp.concatenate([a.reshape(-1).astype(F32) for a in arrays])
    rows = -(-flat.shape[0] // LANES)
    rows = -(-rows // SUBLANES) * SUBLANES
    flat = jnp.pad(flat, (0, rows * LANES - flat.shape[0]))
    return flat.reshape(rows, LANES)


def _local_step(xs, target, P, late_weights, on_grad):
    S, D = xs.shape
    qkv_width = 3 * N_HEADS * HEAD_DIM
    glu_col0, gate_col0 = qkv_width, qkv_width + 2 * D
    shard_major = lambda g: g.reshape(N_CHIPS, g.shape[0] // N_CHIPS, g.shape[1])

    h1 = _rms_fwd_call(xs, P["norm_mix_pre"])
    buckets = _bucket_tables()
    bias = _bias_table_call(P["rel_bias"] + 0.0 * h1[0, 0].astype(F32), buckets)
    P = dict(P, **late_weights("in", bias))
    proj = _matmul(h1, P["w_in"], "nn", "proj_in")
    parts = []
    for g in range(N_GROUPS):
        parts += _attn_fwd_call(proj, bias, g)
    a, a_bf, lse = _attn_merge_call(parts)
    P = dict(P, **late_weights("mix", a_bf))
    y_a = _matmul(a_bf, P["w_attn_out"], "nn", "attn_out")
    c1 = _conv_fwd_call(proj, glu_col0, P["conv_dw_w"], P["conv_dw_b"])
    cact = _ln_silu_call(c1, P["conv_ln_g"], P["conv_ln_b"])
    y_c = _matmul(cact, P["conv_pw_w"], "nn", "conv_pw")
    mixed = _mix_call(proj, gate_col0, P["b_gate"], y_a, y_c)
    out = _matmul(mixed, P["w_out"], "nn", "mix_out")
    x1, h2 = _res1_call(xs, out, P["norm_mix_post"], P["norm_ffn_pre"])
    P = dict(P, **late_weights("up", h2))
    u = _matmul(h2, P["w_up"], "nn", "ffn_up")
    f = _ffn_fwd_call(u, P["ffn_conv_w"], P["ffn_conv_b"])
    P = dict(P, **late_weights("down", f))
    yff = _matmul(f, P["w_down"], "nn", "ffn_down")
    loss_tile, dx2, dyff, dg_ffn_post = _loss_call(yff, x1, P["norm_ffn_post"], target)

    G = {}
    G["norm_ffn_post"] = dg_ffn_post
    on_grad("w_down", shard_major(_matmul(f, dyff, "tn", "ffn_down_dw")))
    df = _matmul(dyff, P["w_down"], "nt", "ffn_down_dx")
    du, dwg, dwv, dbg, dbv = _ffn_bwd_call(u, P["ffn_conv_w"], P["ffn_conv_b"], df)
    G["ffn_conv_w"] = jnp.concatenate([dwg[:FFN_CONV_WIDTH], dwv[:FFN_CONV_WIDTH]], axis=1)
    G["ffn_conv_b"] = jnp.concatenate([dbg, dbv], axis=1)
    du = on_grad("w_up", functools.partial(_grad_half_matmul, h2, "ffn_up_dw"), carry=du)
    dh2 = _matmul(du, P["w_up"], "nt", "ffn_up_dx")
    dx1, dout, G["norm_ffn_pre"], G["norm_mix_post"] = _mid_bwd_call(x1, P["norm_ffn_pre"], dh2, dx2, out, P["norm_mix_post"])
    on_grad("w_out", shard_major(_matmul(mixed, dout, "tn", "mix_out_dw")))
    dmixed = _matmul(dout, P["w_out"], "nt", "mix_out_dx")
    dya, dyc, dproj, dba, dbc = _mix_bwd_call(dmixed, proj, gate_col0, P["b_gate"], y_a, y_c)
    G["b_gate"] = jnp.concatenate([dba, dbc], axis=1)
    on_grad("w_attn_out", _matmul(a_bf, dya, "tn", "attn_out_dw", out_shards=True))
    dyc = on_grad("conv_pw_w", shard_major(_matmul(cact, dyc, "tn", "conv_pw_dw")), carry=dyc)
    da = _matmul(dya, P["w_attn_out"], "nt", "attn_out_dx")
    dcact = _matmul(dyc, P["conv_pw_w"], "nt", "conv_pw_dx")
    dc1, G["conv_ln_g"], G["conv_ln_b"] = _ln_silu_bwd_call(c1, P["conv_ln_g"], P["conv_ln_b"], dcact)
    dproj, dw_dw, G["conv_dw_b"] = _conv_bwd_call(proj, glu_col0, P["conv_dw_w"], dc1, dproj)
    G["conv_dw_w"] = dw_dw[:CONV_WIDTH]
    delta = _attn_delta_call(a, da)
    dbs = []
    for g in range(N_GROUPS):
        dproj, db = _attn_bwd_call(proj, bias, da, lse, delta, g, dproj)
        dbs.append(db)
    G["rel_bias"] = _bias_grad_call(jnp.concatenate(dbs, axis=0), buckets)
    dproj = on_grad("w_in", functools.partial(_grad_half_matmul, h1, "proj_in_dw"), carry=dproj)
    dh1 = _matmul(dproj, P["w_in"], "nt", "proj_in_dx")
    dh1 = on_grad(None, None, carry=dh1)
    grad_x, G["norm_mix_pre"] = _in_bwd_call(xs, P["norm_mix_pre"], dh1, dx1)
    return loss_tile, grad_x, G


def kernel(x, w_in, b_gate, rel_bias, w_attn_out, conv_dw_w, conv_dw_b, conv_ln_g, conv_ln_b, conv_pw_w, w_out, norm_mix_pre, norm_mix_post, norm_ffn_pre, norm_ffn_post, w_up, ffn_conv_w, ffn_conv_b, w_down, loss_target, m_w_in, m_b_gate, m_rel_bias, m_w_attn_out, m_conv_dw_w, m_conv_dw_b, m_conv_ln_g, m_conv_ln_b, m_conv_pw_w, m_w_out, m_norm_mix_pre, m_norm_mix_post, m_norm_ffn_pre, m_norm_ffn_post, m_w_up, m_ffn_conv_w, m_ffn_conv_b, m_w_down, v_w_in, v_b_gate, v_rel_bias, v_w_attn_out, v_conv_dw_w, v_conv_dw_b, v_conv_ln_g, v_conv_ln_b, v_conv_pw_w, v_w_out, v_norm_mix_pre, v_norm_mix_post, v_norm_ffn_pre, v_norm_ffn_post, v_w_up, v_ffn_conv_w, v_ffn_conv_b, v_w_down):
    weights = dict(w_in=w_in, b_gate=b_gate, rel_bias=rel_bias, w_attn_out=w_attn_out, conv_dw_w=conv_dw_w, conv_dw_b=conv_dw_b,
                   conv_ln_g=conv_ln_g, conv_ln_b=conv_ln_b, conv_pw_w=conv_pw_w, w_out=w_out, norm_mix_pre=norm_mix_pre,
                   norm_mix_post=norm_mix_post, norm_ffn_pre=norm_ffn_pre, norm_ffn_post=norm_ffn_post, w_up=w_up,
                   ffn_conv_w=ffn_conv_w, ffn_conv_b=ffn_conv_b, w_down=w_down)
    m_in = dict(w_in=m_w_in, b_gate=m_b_gate, rel_bias=m_rel_bias, w_attn_out=m_w_attn_out, conv_dw_w=m_conv_dw_w,
                conv_dw_b=m_conv_dw_b, conv_ln_g=m_conv_ln_g, conv_ln_b=m_conv_ln_b, conv_pw_w=m_conv_pw_w, w_out=m_w_out,
                norm_mix_pre=m_norm_mix_pre, norm_mix_post=m_norm_mix_post, norm_ffn_pre=m_norm_ffn_pre,
                norm_ffn_post=m_norm_ffn_post, w_up=m_w_up, ffn_conv_w=m_ffn_conv_w, ffn_conv_b=m_ffn_conv_b, w_down=m_w_down)
    v_in = dict(w_in=v_w_in, b_gate=v_b_gate, rel_bias=v_rel_bias, w_attn_out=v_w_attn_out, conv_dw_w=v_conv_dw_w,
                conv_dw_b=v_conv_dw_b, conv_ln_g=v_conv_ln_g, conv_ln_b=v_conv_ln_b, conv_pw_w=v_conv_pw_w, w_out=v_w_out,
                norm_mix_pre=v_norm_mix_pre, norm_mix_post=v_norm_mix_post, norm_ffn_pre=v_norm_ffn_pre,
                norm_ffn_post=v_norm_ffn_post, w_up=v_w_up, ffn_conv_w=v_ffn_conv_w, ffn_conv_b=v_ffn_conv_b, w_down=v_w_down)
    names = list(weights)
    xi, yi, ci = _position()
    chip = 2 * xi + yi
    core_arr = jnp.reshape(ci, (1,)).astype(jnp.int32)

    xs = x[0]
    target = loss_target[0]
    S, D = xs.shape

    big = ["w_in", "w_attn_out", "conv_pw_w", "w_out", "w_up", "w_down"]
    row_sharded = ("conv_pw_w", "w_out", "w_down")
    natural = lambda k, g: g.reshape(-1, g.shape[2]) if k in row_sharded else g
    first_srcs = [w_in[0].astype(BF16), conv_dw_w[0], ffn_conv_w[0]]
    first_lands = [lax.empty((N_CHIPS,) + s.shape, s.dtype) for s in first_srcs]
    (first_hop,) = _split_start("gather_in_start", [(first_srcs, first_lands, 4 * len(first_srcs), _first_hop_copies)], core_arr)
    launched = first_hop["token"]
    late_sets = dict(mix=["w_attn_out", "conv_pw_w", "w_out"], up=["w_up"], down=["w_down"])
    late_groups = []
    for keys in late_sets.values():
        srcs = [(weights[k][0] + launched).astype(BF16) for k in keys]
        late_groups.append((srcs, [lax.empty((N_CHIPS,) + s.shape, BF16) for s in srcs], 4 * len(keys), _gather_copies))
    started = {}

    def late_weights(tag, after):
        if tag == "in":
            casts = [s for srcs, _, _, _ in late_groups for s in srcs]
            w_in_halves, dw4, fc4 = _split_wait("gather_in_wait", first_hop, _first_hop_copies, [after] + casts)[len(first_srcs):]
            second_hop, *late = _split_start("gather_in_pass_start", [([], [w_in_halves], 3, _second_hop_copies)] + late_groups, dw4)
            started.update(zip(late_sets, late))
            (w_in_full,) = _split_wait("gather_in_pass_wait", second_hop, _second_hop_copies, second_hop["tile"])
            return dict(w_in=w_in_full, conv_dw_w=jnp.concatenate(list(dw4), axis=1), ffn_conv_w=jnp.concatenate(list(fc4), axis=1))
        landed = _split_wait(f"gather_{tag}_wait", started[tag], _gather_copies, after)[len(late_sets[tag]):]
        return {k: natural(k, g) for k, g in zip(late_sets[tag], landed)}

    chip_core = jnp.stack([chip, ci]).astype(jnp.int32)
    exchanging, pending, second_half = {}, {}, {}

    held = []

    def launch(tag, after, carry=None):
        keys, groups, partial = [], [], {}
        for k in list(exchanging):
            st, copies = exchanging.pop(k)
            gk, r1 = _split_wait(f"sibling_exchange_wait_{k}", st, copies, after)
            if k in second_half:
                partial[k], s16 = second_half.pop(k)(init=r1)
            else:
                partial[k], s16 = _pair_sum_call(gk, r1, chip_core, f"pair_sum_{k}")
            keys.append(k)
            groups.append(([s16], [lax.empty((3,) + s16.shape[1:], BF16)], 3, _exchange_copies))
        fresh = [(k, copies) for k, _, copies in held]
        for _, g3, copies in held:
            rows = g3.shape[1] // 2 if copies is _sibling_copies else g3.shape[1]
            groups.append(([g3], [lax.empty((N_CHIPS, rows, g3.shape[2]), F32)], 1, copies))
        held.clear()
        begun = _split_start(f"grad_exchange_start_{tag}", groups, core_arr, carry, sibling_only=not keys)
        for k, st in zip(keys, begun):
            pending[k] = (partial[k], st)
        for (k, copies), st in zip(fresh, begun[len(keys):]):
            exchanging[k] = (st, copies)
        return begun[0]["carry"]

    others = [k for k in big if k != "w_in"]
    assembling = {}

    def on_grad(k, g, carry=None):
        if k is None:
            carried = launch("last", carry[:SUBLANES, :LANES], carry)
            assembling["others"] = assemble_start(others, carried, "others", carried)
            return assembling["others"]["carry"]
        if callable(g):
            theirs = g(jnp.stack([chip, 1 - ci]).astype(jnp.int32), carry)
            held.append((k, theirs, _sibling_whole_copies))
            carried = launch(k, theirs[0, :SUBLANES, :LANES], carry)
            second_half[k] = functools.partial(g, chip_core, carried)
            return carried
        held.append((k, g, _sibling_copies))
        if k in ("w_down", "w_out", "w_attn_out"):
            return carry
        return launch(k, g[0, :SUBLANES, :LANES], carry)

    def assemble_start(keys, after, tag, carry=None):
        halves = []
        for k in keys:
            s32, st = pending[k]
            recv2 = _split_wait(f"chip_exchange_wait_{k}", st, _exchange_copies, after)[1]
            halves.append(_chip_sum_call(s32, recv2, chip_core, f"chip_sum_{k}"))
        (st,) = _split_start(f"grad_assemble_start_{tag}", [([], halves, len(halves), _assemble_copies)], core_arr, carry,
                             sibling_only=True)
        return st

    def assemble_wait(keys, st, after, tag):
        return dict(zip(keys, _split_wait(f"grad_assemble_wait_{tag}", st, _assemble_copies, after)))

    P = dict(b_gate=b_gate, rel_bias=rel_bias, conv_dw_b=conv_dw_b, conv_ln_g=conv_ln_g, conv_ln_b=conv_ln_b,
             norm_mix_pre=norm_mix_pre + launched, norm_mix_post=norm_mix_post, norm_ffn_pre=norm_ffn_pre,
             norm_ffn_post=norm_ffn_post, ffn_conv_b=ffn_conv_b)
    loss_tile, grad_x, G = _local_step(xs, target, P, late_weights, on_grad)

    small = [k for k in names if k not in big]
    packed = _pack([loss_tile[:1]] + [G[k] for k in small])
    (allsum,) = _split_start("small_grad_allsum_start",
                             [([packed], [jnp.zeros((N_DEVICES,) + packed.shape, F32)], N_DEVICES - 1, _allsum_copies)], core_arr)

    reduced, grads, deltas, new_m, new_v = {}, {}, {}, {}, {}

    def update(keys):
        for k in keys:
            gk, d, mn, vn = _adamw_call(weights[k][0], reduced[k], m_in[k][0], v_in[k][0], f"adamw_{k}")
            grads[k], deltas[k], new_m[k], new_v[k] = gk[None], d[None], mn[None], vn[None]

    reduced.update(assemble_wait(others, assembling["others"], [allsum["tile"], grad_x], "others"))
    update(others)
    assembling["w_in"] = assemble_start(["w_in"], [deltas[k] for k in others], "w_in")

    me_chip = jnp.stack([4 * xi + 2 * yi + ci, chip]).astype(jnp.int32)
    mine, landed = _split_wait("small_grad_allsum_wait", allsum, _allsum_copies, assembling["w_in"]["tile"])
    piece_shapes = [(1, LANES)] + [(G[k].size // LANES, LANES) if k == "rel_bias" else G[k].shape for k in small]
    piece_cols = [0] + [weights[k].shape[2] if k in ("conv_dw_w", "ffn_conv_w") else 0 for k in small]
    loss_row, *summed = _ordered_sum_call(mine, landed, me_chip, piece_shapes, piece_cols)
    loss = loss_row[0, 0]
    for k, gsum in zip(small, summed):
        grads[k] = gsum.reshape(weights[k].shape)
    ds, mns, vns = _adamw_small_call([weights[k] for k in small], [grads[k] for k in small],
                                     [m_in[k] for k in small], [v_in[k] for k in small])
    deltas.update(zip(small, ds))
    new_m.update(zip(small, mns))
    new_v.update(zip(small, vns))
    reduced.update(assemble_wait(["w_in"], assembling["w_in"], list(ds), "w_in"))
    update(["w_in"])

    return (loss, grad_x[None], *[grads[k] for k in names], *[deltas[k] for k in names],
            *[new_m[k] for k in names], *[new_v[k] for k in names])
```

```python
import functools
import math

import jax
import jax.numpy as jnp
import numpy as np
from jax import lax
from jax.experimental import pallas as pl
from jax.experimental.pallas import tpu as pltpu

F32 = jnp.float32
BF16 = jnp.bfloat16
MESH = pl.DeviceIdType.MESH

HEAD_DIM = 128
HEADS_PER_GROUP = 4
DILATED_PATTERNS = ((128, 1), (512, 4), (2048, 16))
N_GROUPS = 3
N_HEADS = N_GROUPS * HEADS_PER_GROUP
SPAN = 128
GROUP_WIDTH = HEADS_PER_GROUP * HEAD_DIM
CONV_WIDTH = 31
FFN_CONV_WIDTH = 3
N_BUCKETS = 32
MAX_DISTANCE = 2048
RMS_EPS = 1e-6
LN_EPS = 1e-5
NEG_INF = -1e30
ADAM_LR = 0.001
ADAM_B1 = 0.9
ADAM_B2 = 0.999
ADAM_EPS = 1e-08
ADAM_WD = 0.01
ADAM_STEP = 10

LANES = 128
SUBLANES = 8
PACKED_ROWS = 16
ROW_TILE = 512
GATE_ROWS, GATE_COLS = 512, 512
TIME_BLOCK = 128
CONV_PAD = 32
FFN_PAD = 8
VMEM_LIMIT = 56 << 20


def _params(sem=None, vmem=None):
    kw = {}
    if sem is not None:
        kw["dimension_semantics"] = sem
    if vmem is not None:
        kw["vmem_limit_bytes"] = vmem
    return pltpu.CompilerParams(**kw)


def _pick(n, cands):
    for c in cands:
        if n % c == 0:
            return c
    return n


ELEMENTWISE_TILE_BYTES = 3 << 19


def _row_tile(rows, cols):
    for align in (16, SUBLANES):
        fits = [t for t in range(align, rows + 1, align) if rows % t == 0 and t * cols * 4 <= ELEMENTWISE_TILE_BYTES]
        if fits:
            return max(fits)
    return SUBLANES


N_CHIPS = 4
M_TILES = (1024, 1408, 512, 256, 128)
N_TILES = (1024, 512, 1408, 256, 128)
K_TILES = (2176, 2048, 1408, 1024, 512, 256, 128)


def _matmul(a, b, mode, name, out_shards=False, tm=None):
    assert a.dtype == BF16 and b.dtype == BF16, (name, a.dtype, b.dtype)
    b3 = b.ndim == 3
    tn = tk = None
    halves = None
    if mode == "nn":
        M, K = a.shape
        N = b.shape[-1] * (N_CHIPS if b3 else 1)
        tn = b.shape[-1] if b3 else None
    elif mode == "nt":
        if a.ndim == 3:
            halves = a.shape[2]
        M, K = a.shape[-2], a.shape[-1] * (a.shape[0] if a.ndim == 3 else 1)
        N = b.shape[-2]
        tk = b.shape[-1] if b3 else None
    else:
        if b3:
            halves = b.shape[2]
        K, M = a.shape
        N = b.shape[-1] * (b.shape[0] if b3 else 1)
        tn = N // N_CHIPS if out_shards else None
    tm = tm or _pick(M, M_TILES)
    tn = tn or _pick(N, N_TILES)
    tk = tk or _pick(K, K_TILES)
    nk = K // tk
    dn = {"nn": (((1,), (0,)), ((), ())), "nt": (((1,), (1,)), ((), ())), "tn": (((0,), (0,)), ((), ()))}[mode]

    def body(a_ref, b_ref, o_ref):
        if nk == 1:
            o_ref[...] = lax.dot_general(a_ref[...], b_ref[...], dn, preferred_element_type=F32)
        else:
            @pl.when(pl.program_id(2) == 0)
            def _():
                o_ref[...] = jnp.zeros_like(o_ref)

            o_ref[...] += lax.dot_general(a_ref[...], b_ref[...], dn, preferred_element_type=F32)

    if mode == "tn":
        a_spec = pl.BlockSpec((tk, tm), lambda i, j, k: (k, i))
    elif halves:
        per = halves // tk
        a_spec = pl.BlockSpec((None, tm, tk), lambda i, j, k: (k // per, i, k % per))
    else:
        a_spec = pl.BlockSpec((tm, tk), lambda i, j, k: (i, k))
    if mode == "nn":
        b_spec = pl.BlockSpec((None, tk, tn), lambda i, j, k: (j, k, 0)) if b3 else pl.BlockSpec((tk, tn), lambda i, j, k: (k, j))
    elif mode == "nt":
        b_spec = pl.BlockSpec((None, tn, tk), lambda i, j, k: (k, j, 0)) if b3 else pl.BlockSpec((tn, tk), lambda i, j, k: (j, k))
    elif halves:
        per = halves // tn
        b_spec = pl.BlockSpec((None, tk, tn), lambda i, j, k: (j // per, k, j % per))
    else:
        b_spec = pl.BlockSpec((tk, tn), lambda i, j, k: (k, j))
    if out_shards:
        out_spec = pl.BlockSpec((None, tm, tn), lambda i, j, k: (j, i, 0))
        out_shape = jax.ShapeDtypeStruct((N_CHIPS, M, tn), F32)
    else:
        out_spec = pl.BlockSpec((tm, tn), lambda i, j, k: (i, j))
        out_shape = jax.ShapeDtypeStruct((M, N), F32)
    return pl.pallas_call(
        body, name=name, grid=(M // tm, N // tn, nk),
        in_specs=[a_spec, b_spec], out_specs=out_spec, out_shape=out_shape,
        compiler_params=_params(("parallel", "parallel", "arbitrary"), VMEM_LIMIT),
    )(a, b)


def _grad_half_matmul(a, name, chip_half, b, init=None):
    K, M = a.shape
    parts = b.ndim == 3
    N = b.shape[-1] * (b.shape[0] if parts else 1)
    h, tn = M // 2, N // N_CHIPS
    summed = init is not None
    dn = (((0,), (0,)), ((), ()))

    def body(ch_ref, a_ref, b_ref, *rest):
        product = lax.dot_general(a_ref[...], b_ref[...], dn, preferred_element_type=F32)
        if not summed:
            rest[0][...] = product
            return
        init_ref, own_ref, sum16_ref = rest
        total = product + init_ref[...]
        sum16_ref[...] = total.astype(BF16)

        @pl.when(pl.program_id(0) == ch_ref[0])
        def _():
            own_ref[...] = total

    if parts:
        per = b.shape[2] // tn
        b_spec = pl.BlockSpec((None, K, tn), lambda j, ch_ref: (j // per, 0, j % per))
    else:
        b_spec = pl.BlockSpec((K, tn), lambda j, ch_ref: (0, j))
    shard_spec = pl.BlockSpec((None, h, tn), lambda j, ch_ref: (j, 0, 0))
    own_spec = pl.BlockSpec((h, tn), lambda j, ch_ref: (0, 0))
    shape = (N_CHIPS, h, tn)
    return pl.pallas_call(
        body, name=name + ("_mine" if summed else "_theirs"),
        grid_spec=pltpu.PrefetchScalarGridSpec(
            num_scalar_prefetch=1, grid=(N_CHIPS,),
            in_specs=[pl.BlockSpec((K, h), lambda j, ch_ref: (0, ch_ref[1])), b_spec] + [shard_spec] * summed,
            out_specs=[own_spec, shard_spec] if summed else shard_spec),
        out_shape=[jax.ShapeDtypeStruct((h, tn), F32), jax.ShapeDtypeStruct(shape, BF16)] if summed
        else jax.ShapeDtypeStruct(shape, F32),
        compiler_params=_params(("arbitrary",), VMEM_LIMIT),
    )(chip_half, a, b, *([init] if summed else []))


def _rms(x, g):
    r = lax.rsqrt(jnp.mean(x * x, axis=-1, keepdims=True) + RMS_EPS)
    return x * r * g


def _rms_bwd(x, g, dy):
    r = lax.rsqrt(jnp.mean(x * x, axis=-1, keepdims=True) + RMS_EPS)
    n = x * r
    dn = dy * g
    dx = r * (dn - n * jnp.mean(dn * n, axis=-1, keepdims=True))
    return dx, jnp.sum(dy * n, axis=0, keepdims=True)


def _sigmoid(x):
    return 1.0 / (1.0 + jnp.exp(-x))


_GELU_C = math.sqrt(2.0 / math.pi)


def _gelu(x):
    return 0.5 * x * (1.0 + jnp.tanh(_GELU_C * (x + 0.044715 * x * x * x)))


def _gelu_and_grad(x):
    x2 = x * x
    t = jnp.tanh(_GELU_C * x * (1.0 + 0.044715 * x2))
    half = 0.5 * (1.0 + t)
    return x * half, half + (0.5 * _GELU_C) * x * (1.0 - t * t) * (1.0 + (3.0 * 0.044715) * x2)


def _row_spec(width, col_block=0):
    return pl.BlockSpec((ROW_TILE, width), lambda i: (i, col_block))


def _vec_spec(width, col_block=0):
    return pl.BlockSpec((1, width), lambda i: (0, col_block))


def _accumulate(ref, part):
    @pl.when(pl.program_id(0) == 0)
    def _():
        ref[...] = part

    @pl.when(pl.program_id(0) > 0)
    def _():
        ref[...] += part


def _rms_fwd_call(x, g):
    S, D = x.shape

    def body(x_ref, g_ref, h_ref):
        h_ref[...] = _rms(x_ref[...], g_ref[...]).astype(BF16)

    return pl.pallas_call(
        body, name="rms_mix_pre", grid=(S // ROW_TILE,),
        in_specs=[_row_spec(D), _vec_spec(D)], out_specs=_row_spec(D),
        out_shape=jax.ShapeDtypeStruct((S, D), BF16),
        compiler_params=_params(("parallel",)),
    )(x, g)


def _ln_silu_call(c1, g, b):
    S, C = c1.shape

    def body(c_ref, g_ref, b_ref, o_ref):
        xv = c_ref[...]
        mu = jnp.mean(xv, axis=-1, keepdims=True)
        xc = xv - mu
        var = jnp.mean(xc * xc, axis=-1, keepdims=True)
        z = xc * lax.rsqrt(var + LN_EPS) * g_ref[...] + b_ref[...]
        o_ref[...] = (z * _sigmoid(z)).astype(BF16)

    return pl.pallas_call(
        body, name="conv_ln_silu", grid=(S // ROW_TILE,),
        in_specs=[_row_spec(C), _vec_spec(C), _vec_spec(C)], out_specs=_row_spec(C),
        out_shape=jax.ShapeDtypeStruct((S, C), BF16),
        compiler_params=_params(("parallel",)),
    )(c1, g, b)


def _ln_silu_bwd_call(c1, g, b, dc):
    S, C = c1.shape

    def body(c_ref, g_ref, b_ref, dc_ref, dx_ref, dg_ref, db_ref):
        xv = c_ref[...]
        mu = jnp.mean(xv, axis=-1, keepdims=True)
        xc = xv - mu
        rs = lax.rsqrt(jnp.mean(xc * xc, axis=-1, keepdims=True) + LN_EPS)
        xh = xc * rs
        z = xh * g_ref[...] + b_ref[...]
        sg = _sigmoid(z)
        dz = dc_ref[...] * (sg * (1.0 + z * (1.0 - sg)))
        dxh = dz * g_ref[...]
        dx_ref[...] = rs * (dxh - jnp.mean(dxh, axis=-1, keepdims=True) - xh * jnp.mean(dxh * xh, axis=-1, keepdims=True))
        _accumulate(dg_ref, jnp.sum(dz * xh, axis=0, keepdims=True))
        _accumulate(db_ref, jnp.sum(dz, axis=0, keepdims=True))

    return pl.pallas_call(
        body, name="conv_ln_silu_bwd", grid=(S // ROW_TILE,),
        in_specs=[_row_spec(C), _vec_spec(C), _vec_spec(C), _row_spec(C)],
        out_specs=[_row_spec(C), _vec_spec(C), _vec_spec(C)],
        out_shape=[jax.ShapeDtypeStruct((S, C), F32), jax.ShapeDtypeStruct((1, C), F32), jax.ShapeDtypeStruct((1, C), F32)],
        compiler_params=_params(("arbitrary",)),
    )(c1, g, b, dc)


def _mix_call(proj, gate_col0, b_gate, y_a, y_c):
    S, D = y_a.shape
    w = GATE_COLS
    nc = D // w
    ga0, gc0 = gate_col0 // w, (gate_col0 + D) // w

    def body(ga_ref, gc_ref, ba_ref, bc_ref, ya_ref, yc_ref, o_ref):
        o_ref[...] = (_sigmoid(ga_ref[...] + ba_ref[...]) * ya_ref[...]
                      + _sigmoid(gc_ref[...] + bc_ref[...]) * yc_ref[...]).astype(BF16)

    tile = lambda off: pl.BlockSpec((GATE_ROWS, w), lambda i, j: (i, off + j))
    vec = lambda off: pl.BlockSpec((1, w), lambda i, j: (0, off + j))
    return pl.pallas_call(
        body, name="gate_mix", grid=(S // GATE_ROWS, nc),
        in_specs=[tile(ga0), tile(gc0), vec(0), vec(nc), tile(0), tile(0)],
        out_specs=tile(0), out_shape=jax.ShapeDtypeStruct((S, D), BF16),
        compiler_params=_params(("parallel", "parallel")),
    )(proj, proj, b_gate, b_gate, y_a, y_c)


def _window_stores(stage_ref, slot, dst_ref, rows, cols, sems):
    width = stage_ref.shape[-1]
    return [pltpu.make_async_copy(stage_ref.at[slot, p], dst_ref.at[rows, pl.ds(pl.multiple_of(c, LANES), width)], sems.at[slot, p])
            for p, c in enumerate(cols)]


def _staged_window_stores(stage_ref, dst_ref, sems, step, n_steps, rows, cols, fill):
    slot = step % 2
    copies = lambda s: _window_stores(stage_ref, s, dst_ref, rows, cols, sems)

    @pl.when(step >= 2)
    def _():
        for cp in copies(slot):
            cp.wait()

    fill(slot)
    for cp in copies(slot):
        cp.start()

    @pl.when(step == n_steps - 1)
    def _():
        for s in ([slot, 1 - slot] if n_steps > 1 else [slot]):
            for cp in copies(s):
                cp.wait()


def _mix_bwd_call(dmixed, proj, gate_col0, b_gate, y_a, y_c):
    S, D = y_a.shape
    w = GATE_COLS
    nc = D // w
    nr = S // GATE_ROWS
    ga0, gc0 = gate_col0 // w, (gate_col0 + D) // w

    def body(dm_ref, ga_ref, gc_ref, ba_ref, bc_ref, ya_ref, yc_ref, dya_ref, dyc_ref, dproj_ref, dba_ref, dbc_ref,
             stage_ref, sems):
        j, i = pl.program_id(0), pl.program_id(1)
        dm = dm_ref[...]
        sa = _sigmoid(ga_ref[...] + ba_ref[...])
        sc = _sigmoid(gc_ref[...] + bc_ref[...])
        dya_ref[...] = (dm * sa).astype(BF16)
        dyc_ref[...] = (dm * sc).astype(BF16)
        dga = dm * ya_ref[...] * sa * (1.0 - sa)
        dgc = dm * yc_ref[...] * sc * (1.0 - sc)

        def fill(slot):
            stage_ref[slot, 0] = dga.astype(BF16)
            stage_ref[slot, 1] = dgc.astype(BF16)

        rows = pl.ds(pl.multiple_of(i * GATE_ROWS, GATE_ROWS), GATE_ROWS)
        _staged_window_stores(stage_ref, dproj_ref, sems, j * nr + i, nc * nr, rows,
                              [gate_col0 + j * w, gate_col0 + D + j * w], fill)
        pa = jnp.sum(dga, axis=0, keepdims=True)
        pc = jnp.sum(dgc, axis=0, keepdims=True)

        @pl.when(i == 0)
        def _():
            dba_ref[...] = pa
            dbc_ref[...] = pc

        @pl.when(i > 0)
        def _():
            dba_ref[...] += pa
            dbc_ref[...] += pc

    tile = lambda off: pl.BlockSpec((GATE_ROWS, w), lambda j, i: (i, off + j))
    vec = lambda off: pl.BlockSpec((1, w), lambda j, i: (0, off + j))
    return pl.pallas_call(
        body, name="gate_mix_bwd", grid=(nc, nr),
        in_specs=[tile(0), tile(ga0), tile(gc0), vec(0), vec(nc), tile(0), tile(0)],
        out_specs=[tile(0), tile(0), ANY, vec(0), vec(0)],
        out_shape=[jax.ShapeDtypeStruct((S, D), BF16)] * 2 + [jax.ShapeDtypeStruct((S, proj.shape[1]), BF16)] + [
                   jax.ShapeDtypeStruct((1, D), F32), jax.ShapeDtypeStruct((1, D), F32)],
        scratch_shapes=[pltpu.VMEM((2, 2, GATE_ROWS, w), BF16), pltpu.SemaphoreType.DMA((2, 2))],
        compiler_params=_params(("arbitrary", "arbitrary")),
    )(dmixed, proj, proj, b_gate, b_gate, y_a, y_c)


def _res1_call(x, out, g_post, g_pre):
    S, D = x.shape

    def body(x_ref, o_ref, gp_ref, gq_ref, x1_ref, h2_ref):
        x1 = x_ref[...] + _rms(o_ref[...], gp_ref[...])
        x1_ref[...] = x1
        h2_ref[...] = _rms(x1, gq_ref[...]).astype(BF16)

    return pl.pallas_call(
        body, name="residual_mix", grid=(S // ROW_TILE,),
        in_specs=[_row_spec(D), _row_spec(D), _vec_spec(D), _vec_spec(D)],
        out_specs=[_row_spec(D), _row_spec(D)],
        out_shape=[jax.ShapeDtypeStruct((S, D), F32), jax.ShapeDtypeStruct((S, D), BF16)],
        compiler_params=_params(("parallel",)),
    )(x, out, g_post, g_pre)


def _loss_call(y, x1, g_post, target):
    S, D = y.shape

    def body(y_ref, x1_ref, g_ref, t_ref, loss_ref, dx_ref, dy_ref, dg_ref):
        yv, gv = y_ref[...], g_ref[...]
        err = x1_ref[...] + _rms(yv, gv) - t_ref[...]
        dx2 = err * (1.0 / D)
        dx_ref[...] = dx2
        dy, dg = _rms_bwd(yv, gv, dx2)
        dy_ref[...] = dy.astype(BF16)
        _accumulate(dg_ref, dg)
        part = 0.5 * jnp.sum(jnp.mean(err * err, axis=-1, keepdims=True), axis=0, keepdims=True)
        _accumulate(loss_ref, jnp.broadcast_to(part, (SUBLANES, LANES)))

    return pl.pallas_call(
        body, name="residual_ffn_loss", grid=(S // ROW_TILE,),
        in_specs=[_row_spec(D), _row_spec(D), _vec_spec(D), _row_spec(D)],
        out_specs=[pl.BlockSpec((SUBLANES, LANES), lambda i: (0, 0)), _row_spec(D), _row_spec(D), _vec_spec(D)],
        out_shape=[jax.ShapeDtypeStruct((SUBLANES, LANES), F32), jax.ShapeDtypeStruct((S, D), F32),
                   jax.ShapeDtypeStruct((S, D), BF16), jax.ShapeDtypeStruct((1, D), F32)],
        compiler_params=_params(("arbitrary",)),
    )(y, x1, g_post, target)


def _mid_bwd_call(x1, g_pre, dh2, dx2, out, g_post):
    S, D = x1.shape

    def body(x1_ref, gq_ref, dh_ref, dx2_ref, o_ref, gp_ref, dx1_ref, do_ref, dgq_ref, dgp_ref):
        d, dgq = _rms_bwd(x1_ref[...], gq_ref[...], dh_ref[...])
        dx1 = dx2_ref[...] + d
        dx1_ref[...] = dx1
        do, dgp = _rms_bwd(o_ref[...], gp_ref[...], dx1)
        do_ref[...] = do.astype(BF16)
        _accumulate(dgq_ref, dgq)
        _accumulate(dgp_ref, dgp)

    return pl.pallas_call(
        body, name="residual_mix_bwd", grid=(S // ROW_TILE,),
        in_specs=[_row_spec(D), _vec_spec(D), _row_spec(D), _row_spec(D), _row_spec(D), _vec_spec(D)],
        out_specs=[_row_spec(D), _row_spec(D), _vec_spec(D), _vec_spec(D)],
        out_shape=[jax.ShapeDtypeStruct((S, D), F32), jax.ShapeDtypeStruct((S, D), BF16)] + [jax.ShapeDtypeStruct((1, D), F32)] * 2,
        compiler_params=_params(("arbitrary",)),
    )(x1, g_pre, dh2, dx2, out, g_post)


def _in_bwd_call(x, g, dh1, dx1):
    S, D = x.shape

    def body(x_ref, g_ref, dh_ref, dx1_ref, gx_ref, dg_ref):
        d, dg = _rms_bwd(x_ref[...], g_ref[...], dh_ref[...])
        gx_ref[...] = dx1_ref[...] + d
        _accumulate(dg_ref, dg)

    return pl.pallas_call(
        body, name="rms_mix_pre_bwd", grid=(S // ROW_TILE,),
        in_specs=[_row_spec(D), _vec_spec(D), _row_spec(D), _row_spec(D)],
        out_specs=[_row_spec(D), _vec_spec(D)],
        out_shape=[jax.ShapeDtypeStruct((S, D), F32), jax.ShapeDtypeStruct((1, D), F32)],
        compiler_params=_params(("arbitrary",)),
    )(x, g, dh1, dx1)


def _bucket_table(dilation):
    qi = np.arange(SPAN)[:, None]
    ki = np.arange(2 * SPAN)[None, :]
    dist = np.maximum(qi + SPAN - ki, 0) * dilation
    max_exact = N_BUCKETS // 2
    d = np.maximum(dist, 1).astype(np.float64)
    large = max_exact + (np.log(d / max_exact) / math.log(MAX_DISTANCE / max_exact) * (N_BUCKETS - max_exact)).astype(np.int32)
    large = np.minimum(large, N_BUCKETS - 1)
    return np.where(dist < max_exact, dist, large).astype(np.int32)


def _bucket_tables():
    return jnp.asarray(np.stack([_bucket_table(r) for _, r in DILATED_PATTERNS]))


def _bias_table_call(rel_bias, buckets):
    def body(rb_ref, bk_ref, o_ref):
        for h in range(N_HEADS):
            bk = bk_ref[h // HEADS_PER_GROUP]

            def step(b, acc):
                return jnp.where(bk == b, rb_ref[b, h], acc)

            o_ref[h] = lax.fori_loop(0, N_BUCKETS, step, jnp.zeros((SPAN, 2 * SPAN), F32))

    return pl.pallas_call(
        body, name="rel_bias_table",
        in_specs=[pl.BlockSpec(memory_space=pltpu.SMEM), pl.BlockSpec(memory_space=pltpu.VMEM)],
        out_specs=pl.BlockSpec(memory_space=pltpu.VMEM),
        out_shape=jax.ShapeDtypeStruct((N_HEADS, SPAN, 2 * SPAN), F32),
    )(rel_bias, buckets)


def _bias_grad_call(dbias, buckets):
    def body(db_ref, bk_ref, o_ref, rows_ref):
        for h in range(N_HEADS):
            bk = bk_ref[h // HEADS_PER_GROUP]
            dv = db_ref[h]

            def step(b, carry):
                rows_ref[h, b] = jnp.sum(jnp.where(bk == b, dv, 0.0), axis=0, keepdims=True)
                return carry

            lax.fori_loop(0, N_BUCKETS, step, 0)
        o_ref[...] = jnp.sum(rows_ref[...], axis=-1, keepdims=True)

    out = pl.pallas_call(
        body, name="rel_bias_grad",
        in_specs=[pl.BlockSpec(memory_space=pltpu.VMEM), pl.BlockSpec(memory_space=pltpu.VMEM)],
        out_specs=pl.BlockSpec(memory_space=pltpu.VMEM),
        out_shape=jax.ShapeDtypeStruct((N_HEADS, N_BUCKETS, 1, 1), F32),
        scratch_shapes=[pltpu.VMEM((N_HEADS, N_BUCKETS, 1, 2 * SPAN), F32)],
    )(dbias, buckets)
    return out.reshape(N_HEADS, N_BUCKETS).T


def _dot_nt(a, b):
    return lax.dot_general(a, b, (((1,), (1,)), ((), ())), preferred_element_type=F32)


def _dot_nn(a, b):
    return lax.dot_general(a, b, (((1,), (0,)), ((), ())), preferred_element_type=F32)


def _dot_tn(a, b):
    return lax.dot_general(a, b, (((0,), (0,)), ((), ())), preferred_element_type=F32)


def _band_masks(n, nb):
    qi = lax.broadcasted_iota(jnp.int32, (SPAN, SPAN), 0)
    ki = lax.broadcasted_iota(jnp.int32, (SPAN, SPAN), 1)
    prev_ok = jnp.logical_and(ki >= qi, n > 0)
    cur_ok = ki <= qi
    next_ok = jnp.logical_and(ki >= qi, n < nb - 1)
    return prev_ok, cur_ok, next_ok


def _wide_band_mask(n):
    qi = lax.broadcasted_iota(jnp.int32, (SPAN, 2 * SPAN), 0)
    ki = lax.broadcasted_iota(jnp.int32, (SPAN, 2 * SPAN), 1)
    prev_ok = jnp.logical_and(jnp.logical_and(ki < SPAN, ki >= qi), n > 0)
    cur_ok = jnp.logical_and(ki >= SPAN, ki - SPAN <= qi)
    return jnp.logical_or(prev_ok, cur_ok)


def _attn_plan(S, group):
    r = DILATED_PATTERNS[group][1]
    hp, per = (HEADS_PER_GROUP, 1) if r == 1 else (2, 4)
    return r, S // (r * SPAN), hp, per


def _residue_rows(rho, r):
    return slice(None) if r == 1 else pl.ds(rho, SPAN, stride=r)


def _for_residues(r, per, fn):
    if r == per:
        for u in range(per):
            fn(u)
        return

    def step(i, carry):
        for u in range(per):
            fn(i * per + u)
        return carry

    lax.fori_loop(0, r // per, step, 0)


def _attn_fwd_call(proj, bias, group):
    S = proj.shape[0]
    r, nb, hp, per = _attn_plan(S, group)
    scale = HEAD_DIM ** -0.5
    kinds = ("q", "kp", "kc", "vp", "vc") if nb > 1 else ("q", "kc", "vc")

    per_kind = _refs_per_kind(r, hp)

    def body(*refs):
        ins = {kind: refs[i * per_kind:(i + 1) * per_kind] for i, kind in enumerate(kinds)}
        b_ref, o_ref, lse_ref = refs[len(kinds) * per_kind:]
        n = pl.program_id(1)
        prev_ok, cur_ok, _ = _band_masks(n, nb)

        band_ok = _wide_band_mask(n) if nb > 1 else cur_ok

        def residue(rho):
            rows = _residue_rows(rho, r)
            for j in range(hp):
                get = lambda kind: _head_rows(ins[kind], j, rows, r).astype(BF16)
                q = get("q")
                if nb > 1:
                    keys, vals, bias_j = jnp.concatenate([get("kp"), get("kc")], axis=0), jnp.concatenate([get("vp"), get("vc")], axis=0), b_ref[j]
                else:
                    keys, vals, bias_j = get("kc"), get("vc"), b_ref[j, :, SPAN:]
                s = jnp.where(band_ok, _dot_nt(q, keys) * scale + bias_j, NEG_INF)
                m = jnp.max(s, axis=-1, keepdims=True)
                p = jnp.exp(s - m)
                den = jnp.sum(p, axis=-1, keepdims=True)
                o_ref[j, rows, :] = _dot_nn(p.astype(BF16), vals) / den
                lse_ref[j, rows, :] = jnp.broadcast_to(m + jnp.log(den), (SPAN, HEAD_DIM))

        _for_residues(r, per, residue)

    in_specs = [_head_spec(r, nb, hp, kind, group, jj) for kind in kinds for jj in range(per_kind)]
    in_specs.append(pl.BlockSpec((hp, SPAN, 2 * SPAN), lambda j, n: (group * (HEADS_PER_GROUP // hp) + j, 0, 0)))
    out = pl.BlockSpec((hp, r * SPAN, HEAD_DIM), lambda j, n: (j, n, 0))
    return pl.pallas_call(
        body, name=f"attn_fwd_g{group}", grid=(HEADS_PER_GROUP // hp, nb),
        in_specs=in_specs, out_specs=[out] * 2,
        out_shape=[jax.ShapeDtypeStruct((HEADS_PER_GROUP, S, HEAD_DIM), F32)] * 2,
        compiler_params=_params(("parallel", "parallel"), VMEM_LIMIT),
    )(*([proj] * (len(in_specs) - 1)), bias)


_PROJ_PART = dict(q=0, qn=0, kp=1, kc=1, vp=2, vc=2)


def _refs_per_kind(r, hp):
    return 1 if r == 1 else hp


def _head_rows(refs, j, rows, r):
    return refs[0][:, j * HEAD_DIM:(j + 1) * HEAD_DIM] if r == 1 else refs[j][rows, :]


def _head_spec(r, nb, hp, kind, group, jj):
    if kind in _PROJ_PART:
        base = (_PROJ_PART[kind] * N_GROUPS + group) * HEADS_PER_GROUP
    else:
        base = 0
    if kind.endswith("p"):
        row = lambda n: jnp.maximum(n - 1, 0)
    elif kind.endswith("n"):
        row = lambda n: jnp.minimum(n + 1, nb - 1)
    else:
        row = lambda n: n
    if r == 1:
        return pl.BlockSpec((SPAN, hp * HEAD_DIM), lambda j, n: (row(n), base // hp + j))
    return pl.BlockSpec((r * SPAN, HEAD_DIM), lambda j, n: (row(n), base + j * hp + jj))


def _attn_merge_call(parts):
    S = parts[0].shape[1]

    def body(o1, s1, o2, s2, o3, s3, a_ref, ab_ref, lse_ref):
        for j in range(HEADS_PER_GROUP):
            sl = slice(j * HEAD_DIM, (j + 1) * HEAD_DIM)
            mx = jnp.maximum(jnp.maximum(s1[j], s2[j]), s3[j])
            w1 = jnp.exp(s1[j] - mx)
            w2 = jnp.exp(s2[j] - mx)
            w3 = jnp.exp(s3[j] - mx)
            den = w1 + w2 + w3
            a = (w1 * o1[j] + w2 * o2[j] + w3 * o3[j]) / den
            a_ref[:, sl] = a
            ab_ref[:, sl] = a.astype(BF16)
            lse_ref[:, sl] = mx + jnp.log(den)

    heads = pl.BlockSpec((HEADS_PER_GROUP, ROW_TILE, HEAD_DIM), lambda i: (0, i, 0))
    return pl.pallas_call(
        body, name="attn_merge", grid=(S // ROW_TILE,),
        in_specs=[heads] * 6, out_specs=[_row_spec(GROUP_WIDTH)] * 3,
        out_shape=[jax.ShapeDtypeStruct((S, GROUP_WIDTH), F32), jax.ShapeDtypeStruct((S, GROUP_WIDTH), BF16),
                   jax.ShapeDtypeStruct((S, GROUP_WIDTH), F32)],
        compiler_params=_params(("parallel",)),
    )(*parts)


def _attn_delta_call(a, da):
    S = a.shape[0]

    def body(a_ref, da_ref, d_ref):
        for j in range(HEADS_PER_GROUP):
            sl = slice(j * HEAD_DIM, (j + 1) * HEAD_DIM)
            d = jnp.sum(a_ref[:, sl] * da_ref[:, sl], axis=-1, keepdims=True)
            d_ref[:, sl] = jnp.broadcast_to(d, (ROW_TILE, HEAD_DIM))

    return pl.pallas_call(
        body, name="attn_delta", grid=(S // ROW_TILE,),
        in_specs=[_row_spec(GROUP_WIDTH)] * 2, out_specs=_row_spec(GROUP_WIDTH),
        out_shape=jax.ShapeDtypeStruct((S, GROUP_WIDTH), F32),
        compiler_params=_params(("parallel",)),
    )(a, da)


def _attn_bwd_call(proj, bias, da, lse, delta, group, dproj):
    S = proj.shape[0]
    r, nb, hp, per = _attn_plan(S, group)
    scale = HEAD_DIM ** -0.5
    kinds = ("q", "qn", "kp", "kc", "vp", "vc", "da", "dan", "lse", "lsen", "dl", "dln") if nb > 1 else ("q", "kc", "vc", "da", "lse", "dl")
    source = dict(da=da, dan=da, lse=lse, lsen=lse, dl=delta, dln=delta)

    per_kind = _refs_per_kind(r, hp)
    per_group = HEADS_PER_GROUP // hp
    block_rows = r * SPAN

    def body(*refs):
        ins = {kind: refs[i * per_kind:(i + 1) * per_kind] for i, kind in enumerate(kinds)}
        b_ref, _, dproj_ref, db_ref, stage_ref, sems = refs[len(kinds) * per_kind:][:6]
        strided_ref = None if r == 1 else refs[-1]
        jg, n = pl.program_id(0), pl.program_id(1)
        prev_ok, cur_ok, next_ok = _band_masks(n, nb)

        @pl.when(n == 0)
        def _():
            db_ref[...] = jnp.zeros_like(db_ref)

        band_ok = _wide_band_mask(n) if nb > 1 else cur_ok

        def fill(slot):
            def put(part, j, rows, value):
                if r == 1:
                    stage_ref[slot, part, :, j * HEAD_DIM:(j + 1) * HEAD_DIM] = value.astype(BF16)
                else:
                    strided_ref[part, j, rows, :] = value

            _for_residues(r, per, functools.partial(residue, put))
            if r > 1:
                for part in range(3):
                    for j in range(hp):
                        for t0 in range(0, block_rows, ROW_TILE):
                            stage_ref[slot, part, t0:t0 + ROW_TILE, j * HEAD_DIM:(j + 1) * HEAD_DIM] = (
                                strided_ref[part, j, t0:t0 + ROW_TILE, :].astype(BF16))

        def residue(put, rho):
            rows = _residue_rows(rho, r)
            for j in range(hp):
                get = lambda kind: _head_rows(ins[kind], j, rows, r)
                q = get("q").astype(BF16)
                kc = get("kc").astype(BF16)
                vc = get("vc").astype(BF16)
                dav = get("da").astype(BF16)
                lse_q, dl_q = get("lse"), get("dl")
                if nb == 1:
                    pc = jnp.exp(jnp.where(cur_ok, _dot_nt(q, kc) * scale + b_ref[j, :, SPAN:], NEG_INF) - lse_q)
                    dsc = pc * (_dot_nt(dav, vc) - dl_q)
                    dsc_b = dsc.astype(BF16)
                    dq = _dot_nn(dsc_b, kc)
                    dk = _dot_tn(dsc_b, q)
                    dv = _dot_tn(pc.astype(BF16), dav)
                    db_ref[j, :, SPAN:] += dsc
                else:
                    qn = get("qn").astype(BF16)
                    dan = get("dan").astype(BF16)
                    keys = jnp.concatenate([get("kp").astype(BF16), kc], axis=0)
                    vals = jnp.concatenate([get("vp").astype(BF16), vc], axis=0)
                    wide = lambda t: jnp.concatenate([t, t], axis=1)
                    p = jnp.exp(jnp.where(band_ok, _dot_nt(q, keys) * scale + b_ref[j], NEG_INF) - wide(lse_q))
                    ds = p * (_dot_nt(dav, vals) - wide(dl_q))
                    dq = _dot_nn(ds.astype(BF16), keys)
                    db_ref[j] += ds
                    pn = jnp.exp(jnp.where(next_ok, _dot_nt(qn, kc) * scale + b_ref[j, :, :SPAN], NEG_INF) - get("lsen"))
                    dsn = pn * (_dot_nt(dan, vc) - get("dln"))
                    both = lambda cur_part, next_part: jnp.concatenate([cur_part.astype(BF16), next_part.astype(BF16)], axis=0)
                    dk = _dot_tn(both(ds[:, SPAN:], dsn), jnp.concatenate([q, qn], axis=0))
                    dv = _dot_tn(both(p[:, SPAN:], pn), jnp.concatenate([dav, dan], axis=0))
                put(0, j, rows, dq * scale)
                put(1, j, rows, dk * scale)
                put(2, j, rows, dv)

        cols = [(part * N_GROUPS + group) * GROUP_WIDTH + jg * (hp * HEAD_DIM) for part in range(3)]
        rows = pl.ds(pl.multiple_of(n * block_rows, SPAN), block_rows)
        _staged_window_stores(stage_ref, dproj_ref, sems, jg * nb + n, per_group * nb, rows, cols, fill)

    band = (hp, SPAN, 2 * SPAN)
    in_specs = [_head_spec(r, nb, hp, kind, group, jj) for kind in kinds for jj in range(per_kind)]
    in_specs += [pl.BlockSpec(band, lambda j, n: (group * per_group + j, 0, 0)), ANY]
    operands = [source.get(kind, proj) for kind in kinds for _ in range(per_kind)] + [bias, dproj]
    scratch = [pltpu.VMEM((2, 3, block_rows, hp * HEAD_DIM), BF16), pltpu.SemaphoreType.DMA((2, 3))]
    if r > 1:
        scratch.append(pltpu.VMEM((3, hp, block_rows, HEAD_DIM), F32))
    return pl.pallas_call(
        body, name=f"attn_bwd_g{group}", grid=(per_group, nb),
        in_specs=in_specs,
        out_specs=[ANY, pl.BlockSpec(band, lambda j, n: (j, 0, 0))],
        out_shape=[jax.ShapeDtypeStruct(dproj.shape, BF16), jax.ShapeDtypeStruct((HEADS_PER_GROUP, SPAN, 2 * SPAN), F32)],
        input_output_aliases={len(operands) - 1: 0},
        scratch_shapes=scratch,
        compiler_params=_params(("arbitrary", "arbitrary"), VMEM_LIMIT),
    )(*operands)


def _tap_rows(xpad_ref, t0, k, width, pad):
    return xpad_ref[pl.ds(t0 + (pad - (width - 1 - k)), TIME_BLOCK), :]


def _conv_block(xpad_ref, t0, w_ref, width, pad):
    acc = None
    for k in range(width):
        term = w_ref[k:k + 1, :] * _tap_rows(xpad_ref, t0, k, width, pad)
        acc = term if acc is None else acc + term
    return acc


def _conv_transpose_block(dpad_ref, t0, w_ref, width):
    acc = None
    for k in range(width):
        term = w_ref[k:k + 1, :] * dpad_ref[pl.ds(t0 + (width - 1 - k), TIME_BLOCK), :]
        acc = term if acc is None else acc + term
    return acc


def _conv_weight_grad(xpad_ref, t0, dy, dw_ref, width, pad):
    for k in range(width):
        dw_ref[k:k + 1, :] += jnp.sum(dy * _tap_rows(xpad_ref, t0, k, width, pad), axis=0, keepdims=True)


def _time_loop(S, step, skip_first=0, skip_last=0):
    def it(tb, carry):
        step(pl.multiple_of(tb * TIME_BLOCK, TIME_BLOCK))
        return carry

    lax.fori_loop(skip_first, S // TIME_BLOCK - skip_last, it, 0)


def _fill_head(head_ref, x_ref, pad):
    head_ref[0:pad, :] = jnp.zeros((pad, LANES), F32)
    head_ref[pad:, :] = x_ref[0:TIME_BLOCK, :]


def _fill_tail(tail_ref, x_ref, pad):
    S = x_ref.shape[0]
    tail_ref[0:TIME_BLOCK, :] = x_ref[S - TIME_BLOCK:S, :]
    tail_ref[TIME_BLOCK:, :] = jnp.zeros((pad, LANES), F32)


def _conv_fwd_call(proj, col0, w, b):
    S = proj.shape[0]
    C = w.shape[1]
    nt = C // LANES
    v0, g0 = col0 // LANES, (col0 + C) // LANES

    def body(val_ref, gate_ref, w_ref, b_ref, o_ref, pad_ref):
        pad_ref[0:CONV_PAD, :] = jnp.zeros((CONV_PAD, LANES), F32)
        pad_ref[CONV_PAD:, :] = val_ref[...] * _sigmoid(gate_ref[...])

        def step(t0):
            o_ref[pl.ds(t0, TIME_BLOCK), :] = _conv_block(pad_ref, t0, w_ref, CONV_WIDTH, CONV_PAD) + b_ref[...]

        _time_loop(S, step)

    seq = lambda off: pl.BlockSpec((S, LANES), lambda i: (0, off + i))
    return pl.pallas_call(
        body, name="conv_module", grid=(nt,),
        in_specs=[seq(v0), seq(g0), pl.BlockSpec((CONV_WIDTH, LANES), lambda i: (0, i)), pl.BlockSpec((1, LANES), lambda i: (0, i))],
        out_specs=seq(0), out_shape=jax.ShapeDtypeStruct((S, C), F32),
        scratch_shapes=[pltpu.VMEM((S + CONV_PAD, LANES), F32)],
        compiler_params=_params(("parallel",)),
    )(proj, proj, w, b)


def _conv_bwd_call(proj, col0, w, dc1, dproj):
    S = proj.shape[0]
    C = w.shape[1]
    nt = C // LANES
    v0, g0 = col0 // LANES, (col0 + C) // LANES

    def body(val_ref, gate_ref, w_ref, dy_ref, _, dproj_ref, dw_ref, db_ref, xpad_ref, tail_ref, dwacc_ref, stage_ref, sems):
        i = pl.program_id(0)
        xpad_ref[0:CONV_PAD, :] = jnp.zeros((CONV_PAD, LANES), F32)
        xpad_ref[CONV_PAD:, :] = val_ref[...] * _sigmoid(gate_ref[...])
        _fill_tail(tail_ref, dy_ref, CONV_PAD)
        dwacc_ref[...] = jnp.zeros_like(dwacc_ref)

        def fill(slot):
            def block(t0, dy_src, dy_t0):
                rows = pl.ds(t0, TIME_BLOCK)
                _conv_weight_grad(xpad_ref, t0, dy_ref[rows, :], dwacc_ref, CONV_WIDTH, CONV_PAD)
                dc0 = _conv_transpose_block(dy_src, dy_t0, w_ref, CONV_WIDTH)
                sg = _sigmoid(gate_ref[rows, :])
                stage_ref[slot, 0, rows, :] = (dc0 * sg).astype(BF16)
                stage_ref[slot, 1, rows, :] = (dc0 * val_ref[rows, :] * sg * (1.0 - sg)).astype(BF16)

            _time_loop(S, lambda t0: block(t0, dy_ref, t0), skip_last=1)
            block(S - TIME_BLOCK, tail_ref, 0)

        _staged_window_stores(stage_ref, dproj_ref, sems, i, nt, pl.ds(0, S),
                              [col0 + i * LANES, col0 + C + i * LANES], fill)
        dw_ref[...] = dwacc_ref[...]
        db_ref[...] = jnp.sum(dy_ref[...], axis=0, keepdims=True)

    seq = lambda off: pl.BlockSpec((S, LANES), lambda i: (0, off + i))
    return pl.pallas_call(
        body, name="conv_module_bwd", grid=(nt,),
        in_specs=[seq(v0), seq(g0), pl.BlockSpec((CONV_WIDTH, LANES), lambda i: (0, i)), seq(0), ANY],
        out_specs=[ANY, pl.BlockSpec((CONV_PAD, LANES), lambda i: (0, i)), pl.BlockSpec((1, LANES), lambda i: (0, i))],
        out_shape=[jax.ShapeDtypeStruct(dproj.shape, BF16),
                   jax.ShapeDtypeStruct((CONV_PAD, C), F32), jax.ShapeDtypeStruct((1, C), F32)],
        input_output_aliases={4: 0},
        scratch_shapes=[pltpu.VMEM((S + CONV_PAD, LANES), F32), pltpu.VMEM((TIME_BLOCK + CONV_PAD, LANES), F32),
                        pltpu.VMEM((CONV_PAD, LANES), F32),
                        pltpu.VMEM((2, 2, S, LANES), BF16), pltpu.SemaphoreType.DMA((2, 2))],
        compiler_params=_params(("arbitrary",)),
    )(proj, proj, w, dc1, dproj)


def _ffn_fwd_call(u, w, b):
    S, C2 = u.shape
    C = C2 // 2
    nt = C // LANES

    def body(ug_ref, uv_ref, wg_ref, wv_ref, bg_ref, bv_ref, f_ref, hg_ref, hv_ref):
        _fill_head(hg_ref, ug_ref, FFN_PAD)
        _fill_head(hv_ref, uv_ref, FFN_PAD)

        def block(t0, xg_ref, xv_ref, x_t0, pad):
            cg = _conv_block(xg_ref, x_t0, wg_ref, FFN_CONV_WIDTH, pad) + bg_ref[...]
            cv = _conv_block(xv_ref, x_t0, wv_ref, FFN_CONV_WIDTH, pad) + bv_ref[...]
            f_ref[pl.ds(t0, TIME_BLOCK), :] = (_gelu(cg) * cv).astype(BF16)

        block(0, hg_ref, hv_ref, 0, FFN_PAD)
        _time_loop(S, lambda t0: block(t0, ug_ref, uv_ref, t0, 0), skip_first=1)

    seq = lambda off: pl.BlockSpec((S, LANES), lambda i: (0, off + i))
    wsp = lambda off: pl.BlockSpec((FFN_CONV_WIDTH, LANES), lambda i: (0, off + i))
    bsp = lambda off: pl.BlockSpec((1, LANES), lambda i: (0, off + i))
    return pl.pallas_call(
        body, name="ffn_conv_geglu", grid=(nt,),
        in_specs=[seq(0), seq(nt), wsp(0), wsp(nt), bsp(0), bsp(nt)],
        out_specs=seq(0), out_shape=jax.ShapeDtypeStruct((S, C), BF16),
        scratch_shapes=[pltpu.VMEM((FFN_PAD + TIME_BLOCK, LANES), F32)] * 2,
        compiler_params=_params(("parallel",)),
    )(u, u, w, w, b, b)


def _ffn_bwd_call(u, w, b, df):
    S, C2 = u.shape
    C = C2 // 2
    nt = C // LANES

    def body(ug_ref, uv_ref, wg_ref, wv_ref, bg_ref, bv_ref, df_ref,
             du_ref, dwg_ref, dwv_ref, dbg_ref, dbv_ref,
             hg_ref, hv_ref, dg_ref, dv_ref, dwg_acc, dwv_acc, dbg_acc, dbv_acc):
        zeros = jnp.zeros((FFN_PAD, LANES), F32)
        _fill_head(hg_ref, ug_ref, FFN_PAD)
        _fill_head(hv_ref, uv_ref, FFN_PAD)
        dg_ref[S:, :] = zeros
        dv_ref[S:, :] = zeros
        dwg_acc[...] = jnp.zeros_like(dwg_acc)
        dwv_acc[...] = jnp.zeros_like(dwv_acc)
        dbg_acc[...] = jnp.zeros_like(dbg_acc)
        dbv_acc[...] = jnp.zeros_like(dbv_acc)

        def first(t0, xg_ref, xv_ref, x_t0, pad):
            rows = pl.ds(t0, TIME_BLOCK)
            cg = _conv_block(xg_ref, x_t0, wg_ref, FFN_CONV_WIDTH, pad) + bg_ref[...]
            cv = _conv_block(xv_ref, x_t0, wv_ref, FFN_CONV_WIDTH, pad) + bv_ref[...]
            dfb = df_ref[rows, :]
            gelu, gelu_grad = _gelu_and_grad(cg)
            dcg = dfb * cv * gelu_grad
            dcv = dfb * gelu
            dg_ref[rows, :] = dcg
            dv_ref[rows, :] = dcv
            _conv_weight_grad(xg_ref, x_t0, dcg, dwg_acc, FFN_CONV_WIDTH, pad)
            _conv_weight_grad(xv_ref, x_t0, dcv, dwv_acc, FFN_CONV_WIDTH, pad)
            dbg_acc[...] += jnp.sum(dcg, axis=0, keepdims=True)
            dbv_acc[...] += jnp.sum(dcv, axis=0, keepdims=True)

        def second(t0):
            rows = pl.ds(t0, TIME_BLOCK)
            du_ref[0, rows, :] = _conv_transpose_block(dg_ref, t0, wg_ref, FFN_CONV_WIDTH).astype(BF16)
            du_ref[1, rows, :] = _conv_transpose_block(dv_ref, t0, wv_ref, FFN_CONV_WIDTH).astype(BF16)

        first(0, hg_ref, hv_ref, 0, FFN_PAD)
        _time_loop(S, lambda t0: first(t0, ug_ref, uv_ref, t0, 0), skip_first=1)
        _time_loop(S, second)
        dwg_ref[...] = dwg_acc[...]
        dwv_ref[...] = dwv_acc[...]
        dbg_ref[...] = dbg_acc[...]
        dbv_ref[...] = dbv_acc[...]

    seq = lambda off: pl.BlockSpec((S, LANES), lambda i: (0, off + i))
    wsp = lambda off: pl.BlockSpec((FFN_CONV_WIDTH, LANES), lambda i: (0, off + i))
    bsp = lambda off: pl.BlockSpec((1, LANES), lambda i: (0, off + i))
    return pl.pallas_call(
        body, name="ffn_conv_geglu_bwd", grid=(nt,),
        in_specs=[seq(0), seq(nt), wsp(0), wsp(nt), bsp(0), bsp(nt), seq(0)],
        out_specs=[pl.BlockSpec((2, S, LANES), lambda i: (0, 0, i)),
                   pl.BlockSpec((SUBLANES, LANES), lambda i: (0, i)), pl.BlockSpec((SUBLANES, LANES), lambda i: (0, i)),
                   bsp(0), bsp(0)],
        out_shape=[jax.ShapeDtypeStruct((2, S, C), BF16)] + [jax.ShapeDtypeStruct((SUBLANES, C), F32)] * 2
        + [jax.ShapeDtypeStruct((1, C), F32)] * 2,
        scratch_shapes=[pltpu.VMEM((FFN_PAD + TIME_BLOCK, LANES), F32)] * 2 + [pltpu.VMEM((S + FFN_PAD, LANES), F32)] * 2
        + [pltpu.VMEM((SUBLANES, LANES), F32)] * 2
        + [pltpu.VMEM((1, LANES), F32)] * 2,
        compiler_params=_params(("parallel",)),
    )(u, u, w, w, b, b, df)


def _adamw(w_ref, g_ref, m_ref, v_ref, d_ref, mo_ref, vo_ref):
    gv = g_ref[...]
    mn = ADAM_B1 * m_ref[...] + (1.0 - ADAM_B1) * gv
    vn = ADAM_B2 * v_ref[...] + (1.0 - ADAM_B2) * (gv * gv)
    mo_ref[...] = mn
    vo_ref[...] = vn
    m_hat = mn * (1.0 / (1.0 - ADAM_B1 ** ADAM_STEP))
    v_hat = vn * (1.0 / (1.0 - ADAM_B2 ** ADAM_STEP))
    d_ref[...] = -ADAM_LR * (m_hat / (jnp.sqrt(v_hat) + ADAM_EPS) + ADAM_WD * w_ref[...])


def _adamw_call(w, g, m, v, name, half=None, into=None):
    R, C = w.shape
    rows = R if half is None else R // 2
    tr = _row_tile(rows, C)
    steps = rows // tr

    def body(half_ref, w_ref, g_ref, m_ref, v_ref, *rest):
        go_ref, d_ref, mo_ref, vo_ref = rest[-4:]
        go_ref[...] = g_ref[...]
        _adamw(w_ref, g_ref, m_ref, v_ref, d_ref, mo_ref, vo_ref)

    spec = pl.BlockSpec((tr, C), lambda i, half_ref: (half_ref[0] * steps + i, 0))
    prior = [] if into is None else list(into)
    return pl.pallas_call(
        body, name=name,
        grid_spec=pltpu.PrefetchScalarGridSpec(
            num_scalar_prefetch=1, grid=(steps,),
            in_specs=[spec] * 4 + [ANY] * len(prior), out_specs=[spec] * 4),
        out_shape=[jax.ShapeDtypeStruct((R, C), F32)] * 4,
        input_output_aliases={5 + i: i for i in range(len(prior))},
        compiler_params=_params(("parallel",)),
    )(jnp.zeros((1,), jnp.int32) if half is None else half, w, g, m, v, *prior)


def _adamw_small_call(ws, gs, ms, vs):
    n = len(ws)

    def body(*refs):
        w_refs, g_refs, m_refs, v_refs, d_refs, mo_refs, vo_refs = (refs[i * n:(i + 1) * n] for i in range(7))
        for i in range(n):
            _adamw(w_refs[i], g_refs[i], m_refs[i], v_refs[i], d_refs[i], mo_refs[i], vo_refs[i])

    whole = pl.BlockSpec(memory_space=pltpu.VMEM)
    outs = pl.pallas_call(
        body, name="adamw_small",
        in_specs=[whole] * (4 * n), out_specs=[whole] * (3 * n),
        out_shape=[jax.ShapeDtypeStruct(w.shape, F32) for w in ws] * 3,
    )(*ws, *gs, *ms, *vs)
    return outs[:n], outs[n:2 * n], outs[2 * n:]


def _position():
    return lax.axis_index("x"), lax.axis_index("y"), lax.axis_index("c")


def _chip_peers(x, y):
    return [(x, 1 - y), (1 - x, y), (1 - x, 1 - y)]


def _half_rows(ref, core, rows):
    h = rows // 2
    start = pl.multiple_of(core * h, PACKED_ROWS)
    return ref.at[pl.ds(start, h), :] if len(ref.shape) == 2 else ref.at[:, pl.ds(start, h), :]


def _shard_half(ref, shard, core, rows):
    h = rows // 2
    return ref.at[shard, pl.ds(pl.multiple_of(core * h, PACKED_ROWS), h), :]


ANY = pl.BlockSpec(memory_space=pl.ANY)


def _first_hop_copies(srcs, lands):
    x, y, c = _position()
    chip = 2 * x + y
    targets = [(px, py, c) for px, py in _chip_peers(x, y)] + [(x, y, 1 - c)]
    rows = srcs[0].shape[0]
    out = []
    for i, (s, l) in enumerate(zip(srcs, lands)):
        for k, dev in enumerate(targets):
            if i == 0 and k < 3:
                out.append((_half_rows(s, c, rows), _shard_half(l, chip, c, rows), dev, k))
            else:
                out.append((s, l.at[chip], dev, len(targets) * i + k))
    return out


def _second_hop_copies(srcs, lands):
    x, y, c = _position()
    rows = lands[0].shape[1]
    out = []
    for k, (px, py) in enumerate(_chip_peers(x, y)):
        half = _shard_half(lands[0], 2 * px + py, c, rows)
        out.append((half, half, (x, y, 1 - c), k))
    return out


HBM_SPEC = pl.BlockSpec(memory_space=pltpu.HBM)
SEM_SPEC = pl.BlockSpec(memory_space=pltpu.SEMAPHORE)
DATAFLOW = pltpu.SideEffectType.DATAFLOW_SIDE_EFFECTING


def _in_hbm(a):
    return pltpu.with_memory_space_constraint(a, pltpu.HBM)


SIBLING_HANDSHAKE_IDS = dict(grad_exchange_start_w_up=1, grad_assemble_start_others=2, grad_assemble_start_w_in=3)


def _split_start(name, groups, after, carry=None, sibling_only=False):
    spans, arrays = [], []
    for srcs, lands, _, _ in groups:
        spans.append((len(arrays), len(srcs), len(lands)))
        arrays += list(srcs) + list(lands)
    if carry is not None:
        arrays.append(carry)
    na, ng = len(arrays), len(groups)

    def body(*refs):
        sems, token = refs[na + 1:na + 1 + 2 * ng], refs[-1]
        if sibling_only:
            x, y, c = _position()
            barrier = pltpu.get_barrier_semaphore()
            pl.semaphore_signal(barrier, inc=1, device_id=(x, y, 1 - c), device_id_type=MESH)
            pl.semaphore_wait(barrier, 1)
        for g, (_, _, _, copies) in enumerate(groups):
            off, ns, nl = spans[g]
            for src, dst, dev, idx in copies(refs[off:off + ns], refs[off + ns:off + ns + nl]):
                pltpu.make_async_remote_copy(src_ref=src, dst_ref=dst, send_sem=sems[2 * g].at[idx], recv_sem=sems[2 * g + 1].at[idx],
                                             device_id=dev, device_id_type=MESH).start()
        token[...] = jnp.zeros_like(token)

    outs = pl.pallas_call(
        body, name=name,
        in_specs=[HBM_SPEC] * na + [ANY],
        out_specs=[SEM_SPEC] * (2 * ng) + [HBM_SPEC] * na + [pl.BlockSpec(memory_space=pltpu.VMEM)],
        out_shape=[pltpu.SemaphoreType.DMA((n_sems,)) for _, _, n_sems, _ in groups for _ in range(2)]
        + [pltpu.HBM(a.shape, a.dtype) for a in arrays] + [jax.ShapeDtypeStruct((SUBLANES, LANES), F32)],
        input_output_aliases={i: 2 * ng + i for i in range(na)},
        compiler_params=pltpu.CompilerParams(has_side_effects=DATAFLOW,
                                             collective_id=SIBLING_HANDSHAKE_IDS[name] if sibling_only else None),
    )(*[_in_hbm(a) for a in arrays], after)
    started = []
    for g, (off, ns, nl) in enumerate(spans):
        thru = outs[2 * ng + off:2 * ng + off + ns + nl]
        started.append(dict(send=outs[2 * g], recv=outs[2 * g + 1], srcs=list(thru[:ns]), lands=list(thru[ns:]),
                            tile=outs[-1], token=outs[-1][0, 0], carry=None if carry is None else outs[2 * ng + na - 1]))
    return started


def _split_wait(name, started, copies, after):
    n, m = len(started["srcs"]), len(started["lands"])
    after = list(after) if isinstance(after, (list, tuple)) else [after]

    def body(*refs):
        src_refs, land_refs = refs[:n], refs[n:n + m]
        send_sem, recv_sem = refs[n + m], refs[n + m + 1]
        for src, dst, dev, idx in copies(src_refs, land_refs):
            cp = pltpu.make_async_remote_copy(src_ref=src, dst_ref=dst, send_sem=send_sem.at[idx], recv_sem=recv_sem.at[idx],
                                              device_id=dev, device_id_type=MESH)
            cp.wait_send()
            cp.wait_recv()

    arrays = started["srcs"] + started["lands"]
    outs = pl.pallas_call(
        body, name=name,
        in_specs=[HBM_SPEC] * (n + m) + [SEM_SPEC, SEM_SPEC] + [ANY] * len(after),
        out_specs=[HBM_SPEC] * (n + m),
        out_shape=[pltpu.HBM(a.shape, a.dtype) for a in arrays],
        input_output_aliases={i: i for i in range(n + m)},
        compiler_params=pltpu.CompilerParams(has_side_effects=DATAFLOW),
    )(*arrays, started["send"], started["recv"], *after)
    return list(outs)


def _gather_copies(srcs, lands):
    x, y, c = _position()
    chip = 2 * x + y
    targets = [(px, py, c) for px, py in _chip_peers(x, y)] + [(x, y, 1 - c)]
    return [(s, l.at[chip], dev, len(targets) * i + k) for i, (s, l) in enumerate(zip(srcs, lands)) for k, dev in enumerate(targets)]


def _sibling_copies(srcs, lands):
    x, y, c = _position()
    return [(_half_rows(srcs[0], 1 - c, srcs[0].shape[1]), lands[0], (x, y, 1 - c), 0)]


def _sibling_whole_copies(srcs, lands):
    x, y, c = _position()
    return [(srcs[0], lands[0], (x, y, 1 - c), 0)]


def _exchange_copies(srcs, lands):
    x, y, c = _position()
    return [(srcs[0].at[2 * px + py], lands[0].at[k], (px, py, c), k) for k, (px, py) in enumerate(_chip_peers(x, y))]


def _pair_sum_call(grad, recv, chip_core, name):
    _, h, B = recv.shape
    tr = _row_tile(h, B)

    def body(cc_ref, g_ref, r_ref, o_ref, ob_ref):
        s = g_ref[...] + r_ref[...]
        ob_ref[...] = s.astype(BF16)

        @pl.when(pl.program_id(1) == cc_ref[0])
        def _():
            o_ref[...] = s

    g_spec = pl.BlockSpec((None, tr, B), lambda i, q, cc_ref: (q, cc_ref[1] * (h // tr) + i, 0))
    spec = pl.BlockSpec((None, tr, B), lambda i, q, cc_ref: (q, i, 0))
    own_spec = pl.BlockSpec((tr, B), lambda i, q, cc_ref: (i, 0))
    return pl.pallas_call(
        body, name=name,
        grid_spec=pltpu.PrefetchScalarGridSpec(num_scalar_prefetch=1, grid=(h // tr, N_CHIPS), in_specs=[g_spec, spec],
                                               out_specs=[own_spec, spec]),
        out_shape=[jax.ShapeDtypeStruct((h, B), F32), jax.ShapeDtypeStruct(recv.shape, BF16)],
        compiler_params=_params(("parallel", "arbitrary")),
    )(chip_core, grad, recv)


def _chip_sum_call(partial, recv, chip_core, name):
    _, h, B = recv.shape
    tr = _row_tile(h, B)

    def body(cc_ref, p_ref, r_ref, o_ref):
        o_ref[...] = ((p_ref[...] + r_ref[0].astype(F32)) + r_ref[1].astype(F32)) + r_ref[2].astype(F32)

    return pl.pallas_call(
        body, name=name,
        grid_spec=pltpu.PrefetchScalarGridSpec(
            num_scalar_prefetch=1, grid=(h // tr,),
            in_specs=[pl.BlockSpec((tr, B), lambda i, cc_ref: (i, 0)),
                      pl.BlockSpec((3, tr, B), lambda i, cc_ref: (0, i, 0))],
            out_specs=pl.BlockSpec((tr, B), lambda i, cc_ref: (cc_ref[1] * (h // tr) + i, 0))),
        out_shape=jax.ShapeDtypeStruct((2 * h, B), F32),
        compiler_params=_params(("parallel",)),
    )(chip_core, partial, recv)


def _assemble_copies(srcs, lands):
    x, y, c = _position()
    out = []
    for i, land in enumerate(lands):
        half = _half_rows(land, c, land.shape[0])
        out.append((half, half, (x, y, 1 - c), i))
    return out


N_DEVICES = 8


def _allsum_copies(srcs, lands):
    x, y, c = _position()
    me = 4 * x + 2 * y + c
    out = []
    for k in range(1, N_DEVICES):
        peer = (1 - x if k & 4 else x, 1 - y if k & 2 else y, 1 - c if k & 1 else c)
        out.append((srcs[0], lands[0].at[me], peer, k - 1))
    return out


def _ordered_sum_call(mine, landed, me_chip, shapes, sharded_cols):
    rows = mine.shape[0]
    outs = [(s[0], n) if n else s for s, n in zip(shapes, sharded_cols)]

    def body(mc_ref, x_ref, l_ref, *refs):
        acc_ref = refs[-1]
        acc = jnp.where(mc_ref[0] == 0, x_ref[...], l_ref[0])
        for d in range(1, N_DEVICES):
            acc = acc + jnp.where(mc_ref[0] == d, x_ref[...], l_ref[d])
        acc_ref[...] = acc
        first = 0
        for o_ref, (r, c), n in zip(refs[:-1], shapes, sharded_cols):
            per_row = c // LANES

            def unpack(chip, o_ref=o_ref, r=r, n=n, per_row=per_row, first=first):
                for i in range(r):
                    for j in range((n or per_row * LANES) // LANES):
                        src = first + i * per_row + chip * ((n or 0) // LANES) + j
                        o_ref[i:i + 1, j * LANES:(j + 1) * LANES] = acc_ref[src:src + 1, :]

            if n:
                for q in range(N_CHIPS):
                    pl.when(mc_ref[1] == q)(functools.partial(unpack, q))
            else:
                unpack(0)
            first += r * per_row

    results = pl.pallas_call(
        body, name="small_grad_sum",
        in_specs=[pl.BlockSpec(memory_space=pltpu.SMEM), pl.BlockSpec(memory_space=pltpu.VMEM), pl.BlockSpec(memory_space=pltpu.VMEM)],
        out_specs=[pl.BlockSpec(memory_space=pltpu.VMEM)] * len(outs),
        out_shape=[jax.ShapeDtypeStruct(s, F32) for s in outs],
        scratch_shapes=[pltpu.VMEM((rows, LANES), F32)],
    )(me_chip, mine, landed)
    return results


def _pack(arrays):
    flat = jnp.concatenate([a.reshape(-1).astype(F32) for a in arrays])
    rows = -(-flat.shape[0] // LANES)
    rows = -(-rows // SUBLANES) * SUBLANES
    flat = jnp.pad(flat, (0, rows * LANES - flat.shape[0]))
    return flat.reshape(rows, LANES)


def _local_step(xs, target, P, late_weights, on_grad):
    S, D = xs.shape
    qkv_width = 3 * N_HEADS * HEAD_DIM
    glu_col0, gate_col0 = qkv_width, qkv_width + 2 * D
    shard_major = lambda g: g.reshape(N_CHIPS, g.shape[0] // N_CHIPS, g.shape[1])

    h1 = _rms_fwd_call(xs, P["norm_mix_pre"])
    buckets = _bucket_tables()
    bias = _bias_table_call(P["rel_bias"] + 0.0 * h1[0, 0].astype(F32), buckets)
    P = dict(P, **late_weights("in", bias))
    proj = _matmul(h1, P["w_in"], "nn", "proj_in")
    parts = []
    for g in range(N_GROUPS):
        parts += _attn_fwd_call(proj, bias, g)
    a, a_bf, lse = _attn_merge_call(parts)
    P = dict(P, **late_weights("mix", a_bf))
    y_a = _matmul(a_bf, P["w_attn_out"], "nn", "attn_out")
    c1 = _conv_fwd_call(proj, glu_col0, P["conv_dw_w"], P["conv_dw_b"])
    cact = _ln_silu_call(c1, P["conv_ln_g"], P["conv_ln_b"])
    y_c = _matmul(cact, P["conv_pw_w"], "nn", "conv_pw")
    mixed = _mix_call(proj, gate_col0, P["b_gate"], y_a, y_c)
    out = _matmul(mixed, P["w_out"], "nn", "mix_out")
    x1, h2 = _res1_call(xs, out, P["norm_mix_post"], P["norm_ffn_pre"])
    P = dict(P, **late_weights("up", h2))
    u = _matmul(h2, P["w_up"], "nn", "ffn_up")
    f = _ffn_fwd_call(u, P["ffn_conv_w"], P["ffn_conv_b"])
    P = dict(P, **late_weights("down", f))
    yff = _matmul(f, P["w_down"], "nn", "ffn_down")
    loss_tile, dx2, dyff, dg_ffn_post = _loss_call(yff, x1, P["norm_ffn_post"], target)

    G = {}
    G["norm_ffn_post"] = dg_ffn_post
    on_grad("w_down", shard_major(_matmul(f, dyff, "tn", "ffn_down_dw")))
    df = _matmul(dyff, P["w_down"], "nt", "ffn_down_dx")
    du, dwg, dwv, dbg, dbv = _ffn_bwd_call(u, P["ffn_conv_w"], P["ffn_conv_b"], df)
    G["ffn_conv_w"] = jnp.concatenate([dwg[:FFN_CONV_WIDTH], dwv[:FFN_CONV_WIDTH]], axis=1)
    G["ffn_conv_b"] = jnp.concatenate([dbg, dbv], axis=1)
    du = on_grad("w_up", functools.partial(_grad_half_matmul, h2, "ffn_up_dw"), carry=du)
    dh2 = _matmul(du, P["w_up"], "nt", "ffn_up_dx")
    dx1, dout, G["norm_ffn_pre"], G["norm_mix_post"] = _mid_bwd_call(x1, P["norm_ffn_pre"], dh2, dx2, out, P["norm_mix_post"])
    on_grad("w_out", shard_major(_matmul(mixed, dout, "tn", "mix_out_dw")))
    dmixed = _matmul(dout, P["w_out"], "nt", "mix_out_dx")
    dya, dyc, dproj, dba, dbc = _mix_bwd_call(dmixed, proj, gate_col0, P["b_gate"], y_a, y_c)
    G["b_gate"] = jnp.concatenate([dba, dbc], axis=1)
    on_grad("w_attn_out", _matmul(a_bf, dya, "tn", "attn_out_dw", out_shards=True))
    dyc = on_grad("conv_pw_w", shard_major(_matmul(cact, dyc, "tn", "conv_pw_dw")), carry=dyc)
    da = _matmul(dya, P["w_attn_out"], "nt", "attn_out_dx")
    dcact = _matmul(dyc, P["conv_pw_w"], "nt", "conv_pw_dx")
    dc1, G["conv_ln_g"], G["conv_ln_b"] = _ln_silu_bwd_call(c1, P["conv_ln_g"], P["conv_ln_b"], dcact)
    dproj, dw_dw, G["conv_dw_b"] = _conv_bwd_call(proj, glu_col0, P["conv_dw_w"], dc1, dproj)
    G["conv_dw_w"] = dw_dw[:CONV_WIDTH]
    delta = _attn_delta_call(a, da)
    dbs = []
    for g in range(N_GROUPS):
        dproj, db = _attn_bwd_call(proj, bias, da, lse, delta, g, dproj)
        dbs.append(db)
    G["rel_bias"] = _bias_grad_call(jnp.concatenate(dbs, axis=0), buckets)
    dproj = on_grad("w_in", functools.partial(_grad_half_matmul, h1, "proj_in_dw"), carry=dproj)
    dh1 = _matmul(dproj, P["w_in"], "nt", "proj_in_dx")
    dh1 = on_grad(None, None, carry=dh1)
    grad_x, G["norm_mix_pre"] = _in_bwd_call(xs, P["norm_mix_pre"], dh1, dx1)
    return loss_tile, grad_x, G


def kernel(x, w_in, b_gate, rel_bias, w_attn_out, conv_dw_w, conv_dw_b, conv_ln_g, conv_ln_b, conv_pw_w, w_out, norm_mix_pre, norm_mix_post, norm_ffn_pre, norm_ffn_post, w_up, ffn_conv_w, ffn_conv_b, w_down, loss_target, m_w_in, m_b_gate, m_rel_bias, m_w_attn_out, m_conv_dw_w, m_conv_dw_b, m_conv_ln_g, m_conv_ln_b, m_conv_pw_w, m_w_out, m_norm_mix_pre, m_norm_mix_post, m_norm_ffn_pre, m_norm_ffn_post, m_w_up, m_ffn_conv_w, m_ffn_conv_b, m_w_down, v_w_in, v_b_gate, v_rel_bias, v_w_attn_out, v_conv_dw_w, v_conv_dw_b, v_conv_ln_g, v_conv_ln_b, v_conv_pw_w, v_w_out, v_norm_mix_pre, v_norm_mix_post, v_norm_ffn_pre, v_norm_ffn_post, v_w_up, v_ffn_conv_w, v_ffn_conv_b, v_w_down):
    weights = dict(w_in=w_in, b_gate=b_gate, rel_bias=rel_bias, w_attn_out=w_attn_out, conv_dw_w=conv_dw_w, conv_dw_b=conv_dw_b,
                   conv_ln_g=conv_ln_g, conv_ln_b=conv_ln_b, conv_pw_w=conv_pw_w, w_out=w_out, norm_mix_pre=norm_mix_pre,
                   norm_mix_post=norm_mix_post, norm_ffn_pre=norm_ffn_pre, norm_ffn_post=norm_ffn_post, w_up=w_up,
                   ffn_conv_w=ffn_conv_w, ffn_conv_b=ffn_conv_b, w_down=w_down)
    m_in = dict(w_in=m_w_in, b_gate=m_b_gate, rel_bias=m_rel_bias, w_attn_out=m_w_attn_out, conv_dw_w=m_conv_dw_w,
                conv_dw_b=m_conv_dw_b, conv_ln_g=m_conv_ln_g, conv_ln_b=m_conv_ln_b, conv_pw_w=m_conv_pw_w, w_out=m_w_out,
                norm_mix_pre=m_norm_mix_pre, norm_mix_post=m_norm_mix_post, norm_ffn_pre=m_norm_ffn_pre,
                norm_ffn_post=m_norm_ffn_post, w_up=m_w_up, ffn_conv_w=m_ffn_conv_w, ffn_conv_b=m_ffn_conv_b, w_down=m_w_down)
    v_in = dict(w_in=v_w_in, b_gate=v_b_gate, rel_bias=v_rel_bias, w_attn_out=v_w_attn_out, conv_dw_w=v_conv_dw_w,
                conv_dw_b=v_conv_dw_b, conv_ln_g=v_conv_ln_g, conv_ln_b=v_conv_ln_b, conv_pw_w=v_conv_pw_w, w_out=v_w_out,
                norm_mix_pre=v_norm_mix_pre, norm_mix_post=v_norm_mix_post, norm_ffn_pre=v_norm_ffn_pre,
                norm_ffn_post=v_norm_ffn_post, w_up=v_w_up, ffn_conv_w=v_ffn_conv_w, ffn_conv_b=v_ffn_conv_b, w_down=v_w_down)
    names = list(weights)
    xi, yi, ci = _position()
    chip = 2 * xi + yi
    core_arr = jnp.reshape(ci, (1,)).astype(jnp.int32)

    xs = x[0]
    target = loss_target[0]
    S, D = xs.shape

    big = ["w_in", "w_attn_out", "conv_pw_w", "w_out", "w_up", "w_down"]
    row_sharded = ("conv_pw_w", "w_out", "w_down")
    natural = lambda k, g: g.reshape(-1, g.shape[2]) if k in row_sharded else g
    first_srcs = [w_in[0].astype(BF16), conv_dw_w[0], ffn_conv_w[0]]
    first_lands = [lax.empty((N_CHIPS,) + s.shape, s.dtype) for s in first_srcs]
    (first_hop,) = _split_start("gather_in_start", [(first_srcs, first_lands, 4 * len(first_srcs), _first_hop_copies)], core_arr)
    launched = first_hop["token"]
    late_sets = dict(mix=["w_attn_out", "conv_pw_w", "w_out"], up=["w_up"], down=["w_down"])
    late_groups = []
    for keys in late_sets.values():
        srcs = [(weights[k][0] + launched).astype(BF16) for k in keys]
        late_groups.append((srcs, [lax.empty((N_CHIPS,) + s.shape, BF16) for s in srcs], 4 * len(keys), _gather_copies))
    started = {}

    def late_weights(tag, after):
        if tag == "in":
            casts = [s for srcs, _, _, _ in late_groups for s in srcs]
            w_in_halves, dw4, fc4 = _split_wait("gather_in_wait", first_hop, _first_hop_copies, [after] + casts)[len(first_srcs):]
            second_hop, *late = _split_start("gather_in_pass_start", [([], [w_in_halves], 3, _second_hop_copies)] + late_groups, dw4)
            started.update(zip(late_sets, late))
            (w_in_full,) = _split_wait("gather_in_pass_wait", second_hop, _second_hop_copies, second_hop["tile"])
            return dict(w_in=w_in_full, conv_dw_w=jnp.concatenate(list(dw4), axis=1), ffn_conv_w=jnp.concatenate(list(fc4), axis=1))
        landed = _split_wait(f"gather_{tag}_wait", started[tag], _gather_copies, after)[len(late_sets[tag]):]
        return {k: natural(k, g) for k, g in zip(late_sets[tag], landed)}

    chip_core = jnp.stack([chip, ci]).astype(jnp.int32)
    exchanging, pending, second_half = {}, {}, {}

    held = []

    def launch(tag, after, carry=None):
        keys, groups, partial = [], [], {}
        for k in list(exchanging):
            st, copies = exchanging.pop(k)
            gk, r1 = _split_wait(f"sibling_exchange_wait_{k}", st, copies, after)
            if k in second_half:
                partial[k], s16 = second_half.pop(k)(init=r1)
            else:
                partial[k], s16 = _pair_sum_call(gk, r1, chip_core, f"pair_sum_{k}")
            keys.append(k)
            groups.append(([s16], [lax.empty((3,) + s16.shape[1:], BF16)], 3, _exchange_copies))
        fresh = [(k, copies) for k, _, copies in held]
        for _, g3, copies in held:
            rows = g3.shape[1] // 2 if copies is _sibling_copies else g3.shape[1]
            groups.append(([g3], [lax.empty((N_CHIPS, rows, g3.shape[2]), F32)], 1, copies))
        held.clear()
        begun = _split_start(f"grad_exchange_start_{tag}", groups, core_arr, carry, sibling_only=not keys)
        for k, st in zip(keys, begun):
            pending[k] = (partial[k], st)
        for (k, copies), st in zip(fresh, begun[len(keys):]):
            exchanging[k] = (st, copies)
        return begun[0]["carry"]

    others = [k for k in big if k != "w_in"]
    assembling = {}

    def on_grad(k, g, carry=None):
        if k is None:
            carried = launch("last", carry[:SUBLANES, :LANES], carry)
            assembling["others"] = assemble_start(others, carried, "others", carried)
            return assembling["others"]["carry"]
        if callable(g):
            theirs = g(jnp.stack([chip, 1 - ci]).astype(jnp.int32), carry)
            held.append((k, theirs, _sibling_whole_copies))
            carried = launch(k, theirs[0, :SUBLANES, :LANES], carry)
            second_half[k] = functools.partial(g, chip_core, carried)
            return carried
        held.append((k, g, _sibling_copies))
        if k in ("w_down", "w_out", "w_attn_out"):
            return carry
        return launch(k, g[0, :SUBLANES, :LANES], carry)

    def assemble_start(keys, after, tag, carry=None):
        halves = []
        for k in keys:
            s32, st = pending[k]
            recv2 = _split_wait(f"chip_exchange_wait_{k}", st, _exchange_copies, after)[1]
            halves.append(_chip_sum_call(s32, recv2, chip_core, f"chip_sum_{k}"))
        (st,) = _split_start(f"grad_assemble_start_{tag}", [([], halves, len(halves), _assemble_copies)], core_arr, carry,
                             sibling_only=True)
        return st

    def assemble_wait(keys, st, after, tag):
        return dict(zip(keys, _split_wait(f"grad_assemble_wait_{tag}", st, _assemble_copies, after)))

    P = dict(b_gate=b_gate, rel_bias=rel_bias, conv_dw_b=conv_dw_b, conv_ln_g=conv_ln_g, conv_ln_b=conv_ln_b,
             norm_mix_pre=norm_mix_pre + launched, norm_mix_post=norm_mix_post, norm_ffn_pre=norm_ffn_pre,
             norm_ffn_post=norm_ffn_post, ffn_conv_b=ffn_conv_b)
    loss_tile, grad_x, G = _local_step(xs, target, P, late_weights, on_grad)

    small = [k for k in names if k not in big]
    packed = _pack([loss_tile[:1]] + [G[k] for k in small])
    (allsum,) = _split_start("small_grad_allsum_start",
                             [([packed], [jnp.zeros((N_DEVICES,) + packed.shape, F32)], N_DEVICES - 1, _allsum_copies)], core_arr)

    reduced, grads, deltas, new_m, new_v = {}, {}, {}, {}, {}

    def update(keys):
        for k in keys:
            gk, d, mn, vn = _adamw_call(weights[k][0], reduced[k], m_in[k][0], v_in[k][0], f"adamw_{k}")
            grads[k], deltas[k], new_m[k], new_v[k] = gk[None], d[None], mn[None], vn[None]

    reduced.update(assemble_wait(others, assembling["others"], [allsum["tile"], grad_x], "others"))
    update(others)
    assembling["w_in"] = assemble_start(["w_in"], [deltas[k] for k in others], "w_in")
    w_in_args = (weights["w_in"][0], m_in["w_in"][0], v_in["w_in"][0])
    first = _adamw_call(w_in_args[0], assembling["w_in"]["lands"][0], *w_in_args[1:], "adamw_w_in_mine", half=core_arr)
    reduced.update(assemble_wait(["w_in"], assembling["w_in"], list(first), "w_in"))
    gk, d, mn, vn = _adamw_call(w_in_args[0], reduced["w_in"], *w_in_args[1:], "adamw_w_in_theirs", half=1 - core_arr, into=first)
    grads["w_in"], deltas["w_in"], new_m["w_in"], new_v["w_in"] = gk[None], d[None], mn[None], vn[None]

    me_chip = jnp.stack([4 * xi + 2 * yi + ci, chip]).astype(jnp.int32)
    mine, landed = _split_wait("small_grad_allsum_wait", allsum, _allsum_copies, deltas["w_in"])
    piece_shapes = [(1, LANES)] + [(G[k].size // LANES, LANES) if k == "rel_bias" else G[k].shape for k in small]
    piece_cols = [0] + [weights[k].shape[2] if k in ("conv_dw_w", "ffn_conv_w") else 0 for k in small]
    loss_row, *summed = _ordered_sum_call(mine, landed, me_chip, piece_shapes, piece_cols)
    loss = loss_row[0, 0]
    for k, gsum in zip(small, summed):
        grads[k] = gsum.reshape(weights[k].shape)
    ds, mns, vns = _adamw_small_call([weights[k] for k in small], [grads[k] for k in small],
                                     [m_in[k] for k in small], [v_in[k] for k in small])
    deltas.update(zip(small, ds))
    new_m.update(zip(small, mns))
    new_v.update(zip(small, vns))

    return (loss, grad_x[None], *[grads[k] for k in names], *[deltas[k] for k in names],
            *[new_m[k] for k in names], *[new_v[k] for k in names])
```

```python
import functools
import math

import jax
import jax.numpy as jnp
import numpy as np
from jax import lax
from jax.experimental import pallas as pl
from jax.experimental.pallas import tpu as pltpu

F32 = jnp.float32
BF16 = jnp.bfloat16
MESH = pl.DeviceIdType.MESH

HEAD_DIM = 128
HEADS_PER_GROUP = 4
DILATED_PATTERNS = ((128, 1), (512, 4), (2048, 16))
N_GROUPS = 3
N_HEADS = N_GROUPS * HEADS_PER_GROUP
SPAN = 128
GROUP_WIDTH = HEADS_PER_GROUP * HEAD_DIM
CONV_WIDTH = 31
FFN_CONV_WIDTH = 3
N_BUCKETS = 32
MAX_DISTANCE = 2048
RMS_EPS = 1e-6
LN_EPS = 1e-5
NEG_INF = -1e30
ADAM_LR = 0.001
ADAM_B1 = 0.9
ADAM_B2 = 0.999
ADAM_EPS = 1e-08
ADAM_WD = 0.01
ADAM_STEP = 10

LANES = 128
SUBLANES = 8
PACKED_ROWS = 16
ROW_TILE = 512
GATE_ROWS, GATE_COLS = 512, 512
TIME_BLOCK = 128
CONV_PAD = 32
FFN_PAD = 8
VMEM_LIMIT = 56 << 20


def _params(sem=None, vmem=None):
    kw = {}
    if sem is not None:
        kw["dimension_semantics"] = sem
    if vmem is not None:
        kw["vmem_limit_bytes"] = vmem
    return pltpu.CompilerParams(**kw)


def _pick(n, cands):
    for c in cands:
        if n % c == 0:
            return c
    return n


ELEMENTWISE_TILE_BYTES = 3 << 19


def _row_tile(rows, cols):
    for align in (16, SUBLANES):
        fits = [t for t in range(align, rows + 1, align) if rows % t == 0 and t * cols * 4 <= ELEMENTWISE_TILE_BYTES]
        if fits:
            return max(fits)
    return SUBLANES


N_CHIPS = 4
M_TILES = (1024, 1408, 512, 256, 128)
N_TILES = (1024, 512, 1408, 256, 128)
K_TILES = (2176, 2048, 1408, 1024, 512, 256, 128)


def _matmul(a, b, mode, name, out_shards=False, tm=None):
    assert a.dtype == BF16 and b.dtype == BF16, (name, a.dtype, b.dtype)
    b3 = b.ndim == 3
    tn = tk = None
    halves = None
    if mode == "nn":
        M, K = a.shape
        N = b.shape[-1] * (N_CHIPS if b3 else 1)
        tn = b.shape[-1] if b3 else None
    elif mode == "nt":
        if a.ndim == 3:
            halves = a.shape[2]
        M, K = a.shape[-2], a.shape[-1] * (a.shape[0] if a.ndim == 3 else 1)
        N = b.shape[-2]
        tk = b.shape[-1] if b3 else None
    else:
        if b3:
            halves = b.shape[2]
        K, M = a.shape
        N = b.shape[-1] * (b.shape[0] if b3 else 1)
        tn = N // N_CHIPS if out_shards else None
    tm = tm or _pick(M, M_TILES)
    tn = tn or _pick(N, N_TILES)
    tk = tk or _pick(K, K_TILES)
    nk = K // tk
    dn = {"nn": (((1,), (0,)), ((), ())), "nt": (((1,), (1,)), ((), ())), "tn": (((0,), (0,)), ((), ()))}[mode]

    def body(a_ref, b_ref, o_ref):
        if nk == 1:
            o_ref[...] = lax.dot_general(a_ref[...], b_ref[...], dn, preferred_element_type=F32)
        else:
            @pl.when(pl.program_id(2) == 0)
            def _():
                o_ref[...] = jnp.zeros_like(o_ref)

            o_ref[...] += lax.dot_general(a_ref[...], b_ref[...], dn, preferred_element_type=F32)

    if mode == "tn":
        a_spec = pl.BlockSpec((tk, tm), lambda i, j, k: (k, i))
    elif halves:
        per = halves // tk
        a_spec = pl.BlockSpec((None, tm, tk), lambda i, j, k: (k // per, i, k % per))
    else:
        a_spec = pl.BlockSpec((tm, tk), lambda i, j, k: (i, k))
    if mode == "nn":
        b_spec = pl.BlockSpec((None, tk, tn), lambda i, j, k: (j, k, 0)) if b3 else pl.BlockSpec((tk, tn), lambda i, j, k: (k, j))
    elif mode == "nt":
        b_spec = pl.BlockSpec((None, tn, tk), lambda i, j, k: (k, j, 0)) if b3 else pl.BlockSpec((tn, tk), lambda i, j, k: (j, k))
    elif halves:
        per = halves // tn
        b_spec = pl.BlockSpec((None, tk, tn), lambda i, j, k: (j // per, k, j % per))
    else:
        b_spec = pl.BlockSpec((tk, tn), lambda i, j, k: (k, j))
    if out_shards:
        out_spec = pl.BlockSpec((None, tm, tn), lambda i, j, k: (j, i, 0))
        out_shape = jax.ShapeDtypeStruct((N_CHIPS, M, tn), F32)
    else:
        out_spec = pl.BlockSpec((tm, tn), lambda i, j, k: (i, j))
        out_shape = jax.ShapeDtypeStruct((M, N), F32)
    return pl.pallas_call(
        body, name=name, grid=(M // tm, N // tn, nk),
        in_specs=[a_spec, b_spec], out_specs=out_spec, out_shape=out_shape,
        compiler_params=_params(("parallel", "parallel", "arbitrary"), VMEM_LIMIT),
    )(a, b)


def _grad_half_matmul(a, name, chip_half, b, init=None):
    K, M = a.shape
    parts = b.ndim == 3
    N = b.shape[-1] * (b.shape[0] if parts else 1)
    h, tn = M // 2, N // N_CHIPS
    summed = init is not None
    dn = (((0,), (0,)), ((), ()))

    def body(ch_ref, a_ref, b_ref, *rest):
        product = lax.dot_general(a_ref[...], b_ref[...], dn, preferred_element_type=F32)
        if not summed:
            rest[0][...] = product
            return
        init_ref, own_ref, sum16_ref = rest
        total = product + init_ref[...]
        sum16_ref[...] = total.astype(BF16)

        @pl.when(pl.program_id(0) == ch_ref[0])
        def _():
            own_ref[...] = total

    if parts:
        per = b.shape[2] // tn
        b_spec = pl.BlockSpec((None, K, tn), lambda j, ch_ref: (j // per, 0, j % per))
    else:
        b_spec = pl.BlockSpec((K, tn), lambda j, ch_ref: (0, j))
    shard_spec = pl.BlockSpec((None, h, tn), lambda j, ch_ref: (j, 0, 0))
    own_spec = pl.BlockSpec((h, tn), lambda j, ch_ref: (0, 0))
    shape = (N_CHIPS, h, tn)
    return pl.pallas_call(
        body, name=name + ("_mine" if summed else "_theirs"),
        grid_spec=pltpu.PrefetchScalarGridSpec(
            num_scalar_prefetch=1, grid=(N_CHIPS,),
            in_specs=[pl.BlockSpec((K, h), lambda j, ch_ref: (0, ch_ref[1])), b_spec] + [shard_spec] * summed,
            out_specs=[own_spec, shard_spec] if summed else shard_spec),
        out_shape=[jax.ShapeDtypeStruct((h, tn), F32), jax.ShapeDtypeStruct(shape, BF16)] if summed
        else jax.ShapeDtypeStruct(shape, F32),
        compiler_params=_params(("arbitrary",), VMEM_LIMIT),
    )(chip_half, a, b, *([init] if summed else []))


def _rms(x, g):
    r = lax.rsqrt(jnp.mean(x * x, axis=-1, keepdims=True) + RMS_EPS)
    return x * r * g


def _rms_bwd(x, g, dy):
    r = lax.rsqrt(jnp.mean(x * x, axis=-1, keepdims=True) + RMS_EPS)
    n = x * r
    dn = dy * g
    dx = r * (dn - n * jnp.mean(dn * n, axis=-1, keepdims=True))
    return dx, jnp.sum(dy * n, axis=0, keepdims=True)


def _sigmoid(x):
    return 1.0 / (1.0 + jnp.exp(-x))


_GELU_C = math.sqrt(2.0 / math.pi)


def _gelu(x):
    return 0.5 * x * (1.0 + jnp.tanh(_GELU_C * (x + 0.044715 * x * x * x)))


def _gelu_and_grad(x):
    x2 = x * x
    t = jnp.tanh(_GELU_C * x * (1.0 + 0.044715 * x2))
    half = 0.5 * (1.0 + t)
    return x * half, half + (0.5 * _GELU_C) * x * (1.0 - t * t) * (1.0 + (3.0 * 0.044715) * x2)


def _row_spec(width, col_block=0):
    return pl.BlockSpec((ROW_TILE, width), lambda i: (i, col_block))


def _vec_spec(width, col_block=0):
    return pl.BlockSpec((1, width), lambda i: (0, col_block))


def _accumulate(ref, part):
    @pl.when(pl.program_id(0) == 0)
    def _():
        ref[...] = part

    @pl.when(pl.program_id(0) > 0)
    def _():
        ref[...] += part


def _rms_fwd_call(x, g):
    S, D = x.shape

    def body(x_ref, g_ref, h_ref):
        h_ref[...] = _rms(x_ref[...], g_ref[...]).astype(BF16)

    return pl.pallas_call(
        body, name="rms_mix_pre", grid=(S // ROW_TILE,),
        in_specs=[_row_spec(D), _vec_spec(D)], out_specs=_row_spec(D),
        out_shape=jax.ShapeDtypeStruct((S, D), BF16),
        compiler_params=_params(("parallel",)),
    )(x, g)


def _ln_silu_call(c1, g, b):
    S, C = c1.shape

    def body(c_ref, g_ref, b_ref, o_ref):
        xv = c_ref[...]
        mu = jnp.mean(xv, axis=-1, keepdims=True)
        xc = xv - mu
        var = jnp.mean(xc * xc, axis=-1, keepdims=True)
        z = xc * lax.rsqrt(var + LN_EPS) * g_ref[...] + b_ref[...]
        o_ref[...] = (z * _sigmoid(z)).astype(BF16)

    return pl.pallas_call(
        body, name="conv_ln_silu", grid=(S // ROW_TILE,),
        in_specs=[_row_spec(C), _vec_spec(C), _vec_spec(C)], out_specs=_row_spec(C),
        out_shape=jax.ShapeDtypeStruct((S, C), BF16),
        compiler_params=_params(("parallel",)),
    )(c1, g, b)


def _ln_silu_bwd_call(c1, g, b, dc):
    S, C = c1.shape

    def body(c_ref, g_ref, b_ref, dc_ref, dx_ref, dg_ref, db_ref):
        xv = c_ref[...]
        mu = jnp.mean(xv, axis=-1, keepdims=True)
        xc = xv - mu
        rs = lax.rsqrt(jnp.mean(xc * xc, axis=-1, keepdims=True) + LN_EPS)
        xh = xc * rs
        z = xh * g_ref[...] + b_ref[...]
        sg = _sigmoid(z)
        dz = dc_ref[...] * (sg * (1.0 + z * (1.0 - sg)))
        dxh = dz * g_ref[...]
        dx_ref[...] = rs * (dxh - jnp.mean(dxh, axis=-1, keepdims=True) - xh * jnp.mean(dxh * xh, axis=-1, keepdims=True))
        _accumulate(dg_ref, jnp.sum(dz * xh, axis=0, keepdims=True))
        _accumulate(db_ref, jnp.sum(dz, axis=0, keepdims=True))

    return pl.pallas_call(
        body, name="conv_ln_silu_bwd", grid=(S // ROW_TILE,),
        in_specs=[_row_spec(C), _vec_spec(C), _vec_spec(C), _row_spec(C)],
        out_specs=[_row_spec(C), _vec_spec(C), _vec_spec(C)],
        out_shape=[jax.ShapeDtypeStruct((S, C), F32), jax.ShapeDtypeStruct((1, C), F32), jax.ShapeDtypeStruct((1, C), F32)],
        compiler_params=_params(("arbitrary",)),
    )(c1, g, b, dc)


def _mix_call(proj, gate_col0, b_gate, y_a, y_c):
    S, D = y_a.shape
    w = GATE_COLS
    nc = D // w
    ga0, gc0 = gate_col0 // w, (gate_col0 + D) // w

    def body(ga_ref, gc_ref, ba_ref, bc_ref, ya_ref, yc_ref, o_ref):
        o_ref[...] = (_sigmoid(ga_ref[...] + ba_ref[...]) * ya_ref[...]
                      + _sigmoid(gc_ref[...] + bc_ref[...]) * yc_ref[...]).astype(BF16)

    tile = lambda off: pl.BlockSpec((GATE_ROWS, w), lambda i, j: (i, off + j))
    vec = lambda off: pl.BlockSpec((1, w), lambda i, j: (0, off + j))
    return pl.pallas_call(
        body, name="gate_mix", grid=(S // GATE_ROWS, nc),
        in_specs=[tile(ga0), tile(gc0), vec(0), vec(nc), tile(0), tile(0)],
        out_specs=tile(0), out_shape=jax.ShapeDtypeStruct((S, D), BF16),
        compiler_params=_params(("parallel", "parallel")),
    )(proj, proj, b_gate, b_gate, y_a, y_c)


def _window_stores(stage_ref, slot, dst_ref, rows, cols, sems):
    width = stage_ref.shape[-1]
    return [pltpu.make_async_copy(stage_ref.at[slot, p], dst_ref.at[rows, pl.ds(pl.multiple_of(c, LANES), width)], sems.at[slot, p])
            for p, c in enumerate(cols)]


def _staged_window_stores(stage_ref, dst_ref, sems, step, n_steps, rows, cols, fill):
    slot = step % 2
    copies = lambda s: _window_stores(stage_ref, s, dst_ref, rows, cols, sems)

    @pl.when(step >= 2)
    def _():
        for cp in copies(slot):
            cp.wait()

    fill(slot)
    for cp in copies(slot):
        cp.start()

    @pl.when(step == n_steps - 1)
    def _():
        for s in ([slot, 1 - slot] if n_steps > 1 else [slot]):
            for cp in copies(s):
                cp.wait()


def _mix_bwd_call(dmixed, proj, gate_col0, b_gate, y_a, y_c):
    S, D = y_a.shape
    w = GATE_COLS
    nc = D // w
    nr = S // GATE_ROWS
    ga0, gc0 = gate_col0 // w, (gate_col0 + D) // w

    def body(dm_ref, ga_ref, gc_ref, ba_ref, bc_ref, ya_ref, yc_ref, dya_ref, dyc_ref, dproj_ref, dba_ref, dbc_ref,
             stage_ref, sems):
        j, i = pl.program_id(0), pl.program_id(1)
        dm = dm_ref[...]
        sa = _sigmoid(ga_ref[...] + ba_ref[...])
        sc = _sigmoid(gc_ref[...] + bc_ref[...])
        dya_ref[...] = (dm * sa).astype(BF16)
        dyc_ref[...] = (dm * sc).astype(BF16)
        dga = dm * ya_ref[...] * sa * (1.0 - sa)
        dgc = dm * yc_ref[...] * sc * (1.0 - sc)

        def fill(slot):
            stage_ref[slot, 0] = dga.astype(BF16)
            stage_ref[slot, 1] = dgc.astype(BF16)

        rows = pl.ds(pl.multiple_of(i * GATE_ROWS, GATE_ROWS), GATE_ROWS)
        _staged_window_stores(stage_ref, dproj_ref, sems, j * nr + i, nc * nr, rows,
                              [gate_col0 + j * w, gate_col0 + D + j * w], fill)
        pa = jnp.sum(dga, axis=0, keepdims=True)
        pc = jnp.sum(dgc, axis=0, keepdims=True)

        @pl.when(i == 0)
        def _():
            dba_ref[...] = pa
            dbc_ref[...] = pc

        @pl.when(i > 0)
        def _():
            dba_ref[...] += pa
            dbc_ref[...] += pc

    tile = lambda off: pl.BlockSpec((GATE_ROWS, w), lambda j, i: (i, off + j))
    vec = lambda off: pl.BlockSpec((1, w), lambda j, i: (0, off + j))
    return pl.pallas_call(
        body, name="gate_mix_bwd", grid=(nc, nr),
        in_specs=[tile(0), tile(ga0), tile(gc0), vec(0), vec(nc), tile(0), tile(0)],
        out_specs=[tile(0), tile(0), ANY, vec(0), vec(0)],
        out_shape=[jax.ShapeDtypeStruct((S, D), BF16)] * 2 + [jax.ShapeDtypeStruct((S, proj.shape[1]), BF16)] + [
                   jax.ShapeDtypeStruct((1, D), F32), jax.ShapeDtypeStruct((1, D), F32)],
        scratch_shapes=[pltpu.VMEM((2, 2, GATE_ROWS, w), BF16), pltpu.SemaphoreType.DMA((2, 2))],
        compiler_params=_params(("arbitrary", "arbitrary")),
    )(dmixed, proj, proj, b_gate, b_gate, y_a, y_c)


def _res1_call(x, out, g_post, g_pre):
    S, D = x.shape

    def body(x_ref, o_ref, gp_ref, gq_ref, x1_ref, h2_ref):
        x1 = x_ref[...] + _rms(o_ref[...], gp_ref[...])
        x1_ref[...] = x1
        h2_ref[...] = _rms(x1, gq_ref[...]).astype(BF16)

    return pl.pallas_call(
        body, name="residual_mix", grid=(S // ROW_TILE,),
        in_specs=[_row_spec(D), _row_spec(D), _vec_spec(D), _vec_spec(D)],
        out_specs=[_row_spec(D), _row_spec(D)],
        out_shape=[jax.ShapeDtypeStruct((S, D), F32), jax.ShapeDtypeStruct((S, D), BF16)],
        compiler_params=_params(("parallel",)),
    )(x, out, g_post, g_pre)


def _loss_call(y, x1, g_post, target):
    S, D = y.shape

    def body(y_ref, x1_ref, g_ref, t_ref, loss_ref, dx_ref, dy_ref, dg_ref):
        yv, gv = y_ref[...], g_ref[...]
        err = x1_ref[...] + _rms(yv, gv) - t_ref[...]
        dx2 = err * (1.0 / D)
        dx_ref[...] = dx2
        dy, dg = _rms_bwd(yv, gv, dx2)
        dy_ref[...] = dy.astype(BF16)
        _accumulate(dg_ref, dg)
        part = 0.5 * jnp.sum(jnp.mean(err * err, axis=-1, keepdims=True), axis=0, keepdims=True)
        _accumulate(loss_ref, jnp.broadcast_to(part, (SUBLANES, LANES)))

    return pl.pallas_call(
        body, name="residual_ffn_loss", grid=(S // ROW_TILE,),
        in_specs=[_row_spec(D), _row_spec(D), _vec_spec(D), _row_spec(D)],
        out_specs=[pl.BlockSpec((SUBLANES, LANES), lambda i: (0, 0)), _row_spec(D), _row_spec(D), _vec_spec(D)],
        out_shape=[jax.ShapeDtypeStruct((SUBLANES, LANES), F32), jax.ShapeDtypeStruct((S, D), F32),
                   jax.ShapeDtypeStruct((S, D), BF16), jax.ShapeDtypeStruct((1, D), F32)],
        compiler_params=_params(("arbitrary",)),
    )(y, x1, g_post, target)


def _mid_bwd_call(x1, g_pre, dh2, dx2, out, g_post):
    S, D = x1.shape

    def body(x1_ref, gq_ref, dh_ref, dx2_ref, o_ref, gp_ref, dx1_ref, do_ref, dgq_ref, dgp_ref):
        d, dgq = _rms_bwd(x1_ref[...], gq_ref[...], dh_ref[...])
        dx1 = dx2_ref[...] + d
        dx1_ref[...] = dx1
        do, dgp = _rms_bwd(o_ref[...], gp_ref[...], dx1)
        do_ref[...] = do.astype(BF16)
        _accumulate(dgq_ref, dgq)
        _accumulate(dgp_ref, dgp)

    return pl.pallas_call(
        body, name="residual_mix_bwd", grid=(S // ROW_TILE,),
        in_specs=[_row_spec(D), _vec_spec(D), _row_spec(D), _row_spec(D), _row_spec(D), _vec_spec(D)],
        out_specs=[_row_spec(D), _row_spec(D), _vec_spec(D), _vec_spec(D)],
        out_shape=[jax.ShapeDtypeStruct((S, D), F32), jax.ShapeDtypeStruct((S, D), BF16)] + [jax.ShapeDtypeStruct((1, D), F32)] * 2,
        compiler_params=_params(("arbitrary",)),
    )(x1, g_pre, dh2, dx2, out, g_post)


def _in_bwd_call(x, g, dh1, dx1):
    S, D = x.shape

    def body(x_ref, g_ref, dh_ref, dx1_ref, gx_ref, dg_ref):
        d, dg = _rms_bwd(x_ref[...], g_ref[...], dh_ref[...])
        gx_ref[...] = dx1_ref[...] + d
        _accumulate(dg_ref, dg)

    return pl.pallas_call(
        body, name="rms_mix_pre_bwd", grid=(S // ROW_TILE,),
        in_specs=[_row_spec(D), _vec_spec(D), _row_spec(D), _row_spec(D)],
        out_specs=[_row_spec(D), _vec_spec(D)],
        out_shape=[jax.ShapeDtypeStruct((S, D), F32), jax.ShapeDtypeStruct((1, D), F32)],
        compiler_params=_params(("arbitrary",)),
    )(x, g, dh1, dx1)


def _bucket_table(dilation):
    qi = np.arange(SPAN)[:, None]
    ki = np.arange(2 * SPAN)[None, :]
    dist = np.maximum(qi + SPAN - ki, 0) * dilation
    max_exact = N_BUCKETS // 2
    d = np.maximum(dist, 1).astype(np.float64)
    large = max_exact + (np.log(d / max_exact) / math.log(MAX_DISTANCE / max_exact) * (N_BUCKETS - max_exact)).astype(np.int32)
    large = np.minimum(large, N_BUCKETS - 1)
    return np.where(dist < max_exact, dist, large).astype(np.int32)


def _bucket_tables():
    return jnp.asarray(np.stack([_bucket_table(r) for _, r in DILATED_PATTERNS]))


def _bias_table_call(rel_bias, buckets):
    def body(rb_ref, bk_ref, o_ref):
        for h in range(N_HEADS):
            bk = bk_ref[h // HEADS_PER_GROUP]

            def step(b, acc):
                return jnp.where(bk == b, rb_ref[b, h], acc)

            o_ref[h] = lax.fori_loop(0, N_BUCKETS, step, jnp.zeros((SPAN, 2 * SPAN), F32))

    return pl.pallas_call(
        body, name="rel_bias_table",
        in_specs=[pl.BlockSpec(memory_space=pltpu.SMEM), pl.BlockSpec(memory_space=pltpu.VMEM)],
        out_specs=pl.BlockSpec(memory_space=pltpu.VMEM),
        out_shape=jax.ShapeDtypeStruct((N_HEADS, SPAN, 2 * SPAN), F32),
    )(rel_bias, buckets)


def _bias_grad_call(dbias, buckets):
    def body(db_ref, bk_ref, o_ref, rows_ref):
        for h in range(N_HEADS):
            bk = bk_ref[h // HEADS_PER_GROUP]
            dv = db_ref[h]

            def step(b, carry):
                rows_ref[h, b] = jnp.sum(jnp.where(bk == b, dv, 0.0), axis=0, keepdims=True)
                return carry

            lax.fori_loop(0, N_BUCKETS, step, 0)
        o_ref[...] = jnp.sum(rows_ref[...], axis=-1, keepdims=True)

    out = pl.pallas_call(
        body, name="rel_bias_grad",
        in_specs=[pl.BlockSpec(memory_space=pltpu.VMEM), pl.BlockSpec(memory_space=pltpu.VMEM)],
        out_specs=pl.BlockSpec(memory_space=pltpu.VMEM),
        out_shape=jax.ShapeDtypeStruct((N_HEADS, N_BUCKETS, 1, 1), F32),
        scratch_shapes=[pltpu.VMEM((N_HEADS, N_BUCKETS, 1, 2 * SPAN), F32)],
    )(dbias, buckets)
    return out.reshape(N_HEADS, N_BUCKETS).T


def _dot_nt(a, b):
    return lax.dot_general(a, b, (((1,), (1,)), ((), ())), preferred_element_type=F32)


def _dot_nn(a, b):
    return lax.dot_general(a, b, (((1,), (0,)), ((), ())), preferred_element_type=F32)


def _dot_tn(a, b):
    return lax.dot_general(a, b, (((0,), (0,)), ((), ())), preferred_element_type=F32)


def _band_masks(n, nb):
    qi = lax.broadcasted_iota(jnp.int32, (SPAN, SPAN), 0)
    ki = lax.broadcasted_iota(jnp.int32, (SPAN, SPAN), 1)
    prev_ok = jnp.logical_and(ki >= qi, n > 0)
    cur_ok = ki <= qi
    next_ok = jnp.logical_and(ki >= qi, n < nb - 1)
    return prev_ok, cur_ok, next_ok


def _wide_band_mask(n):
    qi = lax.broadcasted_iota(jnp.int32, (SPAN, 2 * SPAN), 0)
    ki = lax.broadcasted_iota(jnp.int32, (SPAN, 2 * SPAN), 1)
    prev_ok = jnp.logical_and(jnp.logical_and(ki < SPAN, ki >= qi), n > 0)
    cur_ok = jnp.logical_and(ki >= SPAN, ki - SPAN <= qi)
    return jnp.logical_or(prev_ok, cur_ok)


def _attn_plan(S, group):
    r = DILATED_PATTERNS[group][1]
    hp, per = (HEADS_PER_GROUP, 1) if r == 1 else (2, 4)
    return r, S // (r * SPAN), hp, per


def _residue_rows(rho, r):
    return slice(None) if r == 1 else pl.ds(rho, SPAN, stride=r)


def _for_residues(r, per, fn):
    if r == per:
        for u in range(per):
            fn(u)
        return

    def step(i, carry):
        for u in range(per):
            fn(i * per + u)
        return carry

    lax.fori_loop(0, r // per, step, 0)


def _attn_fwd_call(proj, bias, group):
    S = proj.shape[0]
    r, nb, hp, per = _attn_plan(S, group)
    scale = HEAD_DIM ** -0.5
    kinds = ("q", "kp", "kc", "vp", "vc") if nb > 1 else ("q", "kc", "vc")

    per_kind = _refs_per_kind(r, hp)

    def body(*refs):
        ins = {kind: refs[i * per_kind:(i + 1) * per_kind] for i, kind in enumerate(kinds)}
        b_ref, o_ref, lse_ref = refs[len(kinds) * per_kind:]
        n = pl.program_id(1)
        prev_ok, cur_ok, _ = _band_masks(n, nb)

        band_ok = _wide_band_mask(n) if nb > 1 else cur_ok

        def residue(rho):
            rows = _residue_rows(rho, r)
            for j in range(hp):
                get = lambda kind: _head_rows(ins[kind], j, rows, r).astype(BF16)
                q = get("q")
                if nb > 1:
                    keys, vals, bias_j = jnp.concatenate([get("kp"), get("kc")], axis=0), jnp.concatenate([get("vp"), get("vc")], axis=0), b_ref[j]
                else:
                    keys, vals, bias_j = get("kc"), get("vc"), b_ref[j, :, SPAN:]
                s = jnp.where(band_ok, _dot_nt(q, keys) * scale + bias_j, NEG_INF)
                m = jnp.max(s, axis=-1, keepdims=True)
                p = jnp.exp(s - m)
                den = jnp.sum(p, axis=-1, keepdims=True)
                o_ref[j, rows, :] = _dot_nn(p.astype(BF16), vals) / den
                lse_ref[j, rows, :] = jnp.broadcast_to(m + jnp.log(den), (SPAN, HEAD_DIM))

        _for_residues(r, per, residue)

    in_specs = [_head_spec(r, nb, hp, kind, group, jj) for kind in kinds for jj in range(per_kind)]
    in_specs.append(pl.BlockSpec((hp, SPAN, 2 * SPAN), lambda j, n: (group * (HEADS_PER_GROUP // hp) + j, 0, 0)))
    out = pl.BlockSpec((hp, r * SPAN, HEAD_DIM), lambda j, n: (j, n, 0))
    return pl.pallas_call(
        body, name=f"attn_fwd_g{group}", grid=(HEADS_PER_GROUP // hp, nb),
        in_specs=in_specs, out_specs=[out] * 2,
        out_shape=[jax.ShapeDtypeStruct((HEADS_PER_GROUP, S, HEAD_DIM), F32)] * 2,
        compiler_params=_params(("parallel", "parallel"), VMEM_LIMIT),
    )(*([proj] * (len(in_specs) - 1)), bias)


_PROJ_PART = dict(q=0, qn=0, kp=1, kc=1, vp=2, vc=2)


def _refs_per_kind(r, hp):
    return 1 if r == 1 else hp


def _head_rows(refs, j, rows, r):
    return refs[0][:, j * HEAD_DIM:(j + 1) * HEAD_DIM] if r == 1 else refs[j][rows, :]


def _head_spec(r, nb, hp, kind, group, jj):
    if kind in _PROJ_PART:
        base = (_PROJ_PART[kind] * N_GROUPS + group) * HEADS_PER_GROUP
    else:
        base = 0
    if kind.endswith("p"):
        row = lambda n: jnp.maximum(n - 1, 0)
    elif kind.endswith("n"):
        row = lambda n: jnp.minimum(n + 1, nb - 1)
    else:
        row = lambda n: n
    if r == 1:
        return pl.BlockSpec((SPAN, hp * HEAD_DIM), lambda j, n: (row(n), base // hp + j))
    return pl.BlockSpec((r * SPAN, HEAD_DIM), lambda j, n: (row(n), base + j * hp + jj))


def _attn_merge_call(parts):
    S = parts[0].shape[1]

    def body(o1, s1, o2, s2, o3, s3, a_ref, ab_ref, lse_ref):
        for j in range(HEADS_PER_GROUP):
            sl = slice(j * HEAD_DIM, (j + 1) * HEAD_DIM)
            mx = jnp.maximum(jnp.maximum(s1[j], s2[j]), s3[j])
            w1 = jnp.exp(s1[j] - mx)
            w2 = jnp.exp(s2[j] - mx)
            w3 = jnp.exp(s3[j] - mx)
            den = w1 + w2 + w3
            a = (w1 * o1[j] + w2 * o2[j] + w3 * o3[j]) / den
            a_ref[:, sl] = a
            ab_ref[:, sl] = a.astype(BF16)
            lse_ref[:, sl] = mx + jnp.log(den)

    heads = pl.BlockSpec((HEADS_PER_GROUP, ROW_TILE, HEAD_DIM), lambda i: (0, i, 0))
    return pl.pallas_call(
        body, name="attn_merge", grid=(S // ROW_TILE,),
        in_specs=[heads] * 6, out_specs=[_row_spec(GROUP_WIDTH)] * 3,
        out_shape=[jax.ShapeDtypeStruct((S, GROUP_WIDTH), F32), jax.ShapeDtypeStruct((S, GROUP_WIDTH), BF16),
                   jax.ShapeDtypeStruct((S, GROUP_WIDTH), F32)],
        compiler_params=_params(("parallel",)),
    )(*parts)


def _attn_delta_call(a, da):
    S = a.shape[0]

    def body(a_ref, da_ref, d_ref):
        for j in range(HEADS_PER_GROUP):
            sl = slice(j * HEAD_DIM, (j + 1) * HEAD_DIM)
            d = jnp.sum(a_ref[:, sl] * da_ref[:, sl], axis=-1, keepdims=True)
            d_ref[:, sl] = jnp.broadcast_to(d, (ROW_TILE, HEAD_DIM))

    return pl.pallas_call(
        body, name="attn_delta", grid=(S // ROW_TILE,),
        in_specs=[_row_spec(GROUP_WIDTH)] * 2, out_specs=_row_spec(GROUP_WIDTH),
        out_shape=jax.ShapeDtypeStruct((S, GROUP_WIDTH), F32),
        compiler_params=_params(("parallel",)),
    )(a, da)


def _attn_bwd_call(proj, bias, da, lse, delta, group, dproj):
    S = proj.shape[0]
    r, nb, hp, per = _attn_plan(S, group)
    scale = HEAD_DIM ** -0.5
    kinds = ("q", "qn", "kp", "kc", "vp", "vc", "da", "dan", "lse", "lsen", "dl", "dln") if nb > 1 else ("q", "kc", "vc", "da", "lse", "dl")
    source = dict(da=da, dan=da, lse=lse, lsen=lse, dl=delta, dln=delta)

    per_kind = _refs_per_kind(r, hp)
    per_group = HEADS_PER_GROUP // hp
    block_rows = r * SPAN

    def body(*refs):
        ins = {kind: refs[i * per_kind:(i + 1) * per_kind] for i, kind in enumerate(kinds)}
        b_ref, _, dproj_ref, db_ref, stage_ref, sems = refs[len(kinds) * per_kind:][:6]
        strided_ref = None if r == 1 else refs[-1]
        jg, n = pl.program_id(0), pl.program_id(1)
        prev_ok, cur_ok, next_ok = _band_masks(n, nb)

        @pl.when(n == 0)
        def _():
            db_ref[...] = jnp.zeros_like(db_ref)

        band_ok = _wide_band_mask(n) if nb > 1 else cur_ok

        def fill(slot):
            def put(part, j, rows, value):
                if r == 1:
                    stage_ref[slot, part, :, j * HEAD_DIM:(j + 1) * HEAD_DIM] = value.astype(BF16)
                else:
                    strided_ref[part, j, rows, :] = value

            _for_residues(r, per, functools.partial(residue, put))
            if r > 1:
                for part in range(3):
                    for j in range(hp):
                        for t0 in range(0, block_rows, ROW_TILE):
                            stage_ref[slot, part, t0:t0 + ROW_TILE, j * HEAD_DIM:(j + 1) * HEAD_DIM] = (
                                strided_ref[part, j, t0:t0 + ROW_TILE, :].astype(BF16))

        def residue(put, rho):
            rows = _residue_rows(rho, r)
            for j in range(hp):
                get = lambda kind: _head_rows(ins[kind], j, rows, r)
                q = get("q").astype(BF16)
                kc = get("kc").astype(BF16)
                vc = get("vc").astype(BF16)
                dav = get("da").astype(BF16)
                lse_q, dl_q = get("lse"), get("dl")
                if nb == 1:
                    pc = jnp.exp(jnp.where(cur_ok, _dot_nt(q, kc) * scale + b_ref[j, :, SPAN:], NEG_INF) - lse_q)
                    dsc = pc * (_dot_nt(dav, vc) - dl_q)
                    dsc_b = dsc.astype(BF16)
                    dq = _dot_nn(dsc_b, kc)
                    dk = _dot_tn(dsc_b, q)
                    dv = _dot_tn(pc.astype(BF16), dav)
                    db_ref[j, :, SPAN:] += dsc
                else:
                    qn = get("qn").astype(BF16)
                    dan = get("dan").astype(BF16)
                    keys = jnp.concatenate([get("kp").astype(BF16), kc], axis=0)
                    vals = jnp.concatenate([get("vp").astype(BF16), vc], axis=0)
                    wide = lambda t: jnp.concatenate([t, t], axis=1)
                    p = jnp.exp(jnp.where(band_ok, _dot_nt(q, keys) * scale + b_ref[j], NEG_INF) - wide(lse_q))
                    ds = p * (_dot_nt(dav, vals) - wide(dl_q))
                    dq = _dot_nn(ds.astype(BF16), keys)
                    db_ref[j] += ds
                    pn = jnp.exp(jnp.where(next_ok, _dot_nt(qn, kc) * scale + b_ref[j, :, :SPAN], NEG_INF) - get("lsen"))
                    dsn = pn * (_dot_nt(dan, vc) - get("dln"))
                    both = lambda cur_part, next_part: jnp.concatenate([cur_part.astype(BF16), next_part.astype(BF16)], axis=0)
                    dk = _dot_tn(both(ds[:, SPAN:], dsn), jnp.concatenate([q, qn], axis=0))
                    dv = _dot_tn(both(p[:, SPAN:], pn), jnp.concatenate([dav, dan], axis=0))
                put(0, j, rows, dq * scale)
                put(1, j, rows, dk * scale)
                put(2, j, rows, dv)

        cols = [(part * N_GROUPS + group) * GROUP_WIDTH + jg * (hp * HEAD_DIM) for part in range(3)]
        rows = pl.ds(pl.multiple_of(n * block_rows, SPAN), block_rows)
        _staged_window_stores(stage_ref, dproj_ref, sems, jg * nb + n, per_group * nb, rows, cols, fill)

    band = (hp, SPAN, 2 * SPAN)
    in_specs = [_head_spec(r, nb, hp, kind, group, jj) for kind in kinds for jj in range(per_kind)]
    in_specs += [pl.BlockSpec(band, lambda j, n: (group * per_group + j, 0, 0)), ANY]
    operands = [source.get(kind, proj) for kind in kinds for _ in range(per_kind)] + [bias, dproj]
    scratch = [pltpu.VMEM((2, 3, block_rows, hp * HEAD_DIM), BF16), pltpu.SemaphoreType.DMA((2, 3))]
    if r > 1:
        scratch.append(pltpu.VMEM((3, hp, block_rows, HEAD_DIM), F32))
    return pl.pallas_call(
        body, name=f"attn_bwd_g{group}", grid=(per_group, nb),
        in_specs=in_specs,
        out_specs=[ANY, pl.BlockSpec(band, lambda j, n: (j, 0, 0))],
        out_shape=[jax.ShapeDtypeStruct(dproj.shape, BF16), jax.ShapeDtypeStruct((HEADS_PER_GROUP, SPAN, 2 * SPAN), F32)],
        input_output_aliases={len(operands) - 1: 0},
        scratch_shapes=scratch,
        compiler_params=_params(("arbitrary", "arbitrary"), VMEM_LIMIT),
    )(*operands)


def _tap_rows(xpad_ref, t0, k, width, pad):
    return xpad_ref[pl.ds(t0 + (pad - (width - 1 - k)), TIME_BLOCK), :]


def _conv_block(xpad_ref, t0, w_ref, width, pad):
    acc = None
    for k in range(width):
        term = w_ref[k:k + 1, :] * _tap_rows(xpad_ref, t0, k, width, pad)
        acc = term if acc is None else acc + term
    return acc


def _conv_transpose_block(dpad_ref, t0, w_ref, width):
    acc = None
    for k in range(width):
        term = w_ref[k:k + 1, :] * dpad_ref[pl.ds(t0 + (width - 1 - k), TIME_BLOCK), :]
        acc = term if acc is None else acc + term
    return acc


def _conv_weight_grad(xpad_ref, t0, dy, dw_ref, width, pad):
    for k in range(width):
        dw_ref[k:k + 1, :] += jnp.sum(dy * _tap_rows(xpad_ref, t0, k, width, pad), axis=0, keepdims=True)


def _time_loop(S, step, skip_first=0, skip_last=0):
    def it(tb, carry):
        step(pl.multiple_of(tb * TIME_BLOCK, TIME_BLOCK))
        return carry

    lax.fori_loop(skip_first, S // TIME_BLOCK - skip_last, it, 0)


def _fill_head(head_ref, x_ref, pad):
    head_ref[0:pad, :] = jnp.zeros((pad, LANES), F32)
    head_ref[pad:, :] = x_ref[0:TIME_BLOCK, :]


def _fill_tail(tail_ref, x_ref, pad):
    S = x_ref.shape[0]
    tail_ref[0:TIME_BLOCK, :] = x_ref[S - TIME_BLOCK:S, :]
    tail_ref[TIME_BLOCK:, :] = jnp.zeros((pad, LANES), F32)


def _conv_fwd_call(proj, col0, w, b):
    S = proj.shape[0]
    C = w.shape[1]
    nt = C // LANES
    v0, g0 = col0 // LANES, (col0 + C) // LANES

    def body(val_ref, gate_ref, w_ref, b_ref, o_ref, pad_ref):
        pad_ref[0:CONV_PAD, :] = jnp.zeros((CONV_PAD, LANES), F32)
        pad_ref[CONV_PAD:, :] = val_ref[...] * _sigmoid(gate_ref[...])

        def step(t0):
            o_ref[pl.ds(t0, TIME_BLOCK), :] = _conv_block(pad_ref, t0, w_ref, CONV_WIDTH, CONV_PAD) + b_ref[...]

        _time_loop(S, step)

    seq = lambda off: pl.BlockSpec((S, LANES), lambda i: (0, off + i))
    return pl.pallas_call(
        body, name="conv_module", grid=(nt,),
        in_specs=[seq(v0), seq(g0), pl.BlockSpec((CONV_WIDTH, LANES), lambda i: (0, i)), pl.BlockSpec((1, LANES), lambda i: (0, i))],
        out_specs=seq(0), out_shape=jax.ShapeDtypeStruct((S, C), F32),
        scratch_shapes=[pltpu.VMEM((S + CONV_PAD, LANES), F32)],
        compiler_params=_params(("parallel",)),
    )(proj, proj, w, b)


def _conv_bwd_call(proj, col0, w, dc1, dproj):
    S = proj.shape[0]
    C = w.shape[1]
    nt = C // LANES
    v0, g0 = col0 // LANES, (col0 + C) // LANES

    def body(val_ref, gate_ref, w_ref, dy_ref, _, dproj_ref, dw_ref, db_ref, xpad_ref, tail_ref, dwacc_ref, stage_ref, sems):
        i = pl.program_id(0)
        xpad_ref[0:CONV_PAD, :] = jnp.zeros((CONV_PAD, LANES), F32)
        xpad_ref[CONV_PAD:, :] = val_ref[...] * _sigmoid(gate_ref[...])
        _fill_tail(tail_ref, dy_ref, CONV_PAD)
        dwacc_ref[...] = jnp.zeros_like(dwacc_ref)

        def fill(slot):
            def block(t0, dy_src, dy_t0):
                rows = pl.ds(t0, TIME_BLOCK)
                _conv_weight_grad(xpad_ref, t0, dy_ref[rows, :], dwacc_ref, CONV_WIDTH, CONV_PAD)
                dc0 = _conv_transpose_block(dy_src, dy_t0, w_ref, CONV_WIDTH)
                sg = _sigmoid(gate_ref[rows, :])
                stage_ref[slot, 0, rows, :] = (dc0 * sg).astype(BF16)
                stage_ref[slot, 1, rows, :] = (dc0 * val_ref[rows, :] * sg * (1.0 - sg)).astype(BF16)

            _time_loop(S, lambda t0: block(t0, dy_ref, t0), skip_last=1)
            block(S - TIME_BLOCK, tail_ref, 0)

        _staged_window_stores(stage_ref, dproj_ref, sems, i, nt, pl.ds(0, S),
                              [col0 + i * LANES, col0 + C + i * LANES], fill)
        dw_ref[...] = dwacc_ref[...]
        db_ref[...] = jnp.sum(dy_ref[...], axis=0, keepdims=True)

    seq = lambda off: pl.BlockSpec((S, LANES), lambda i: (0, off + i))
    return pl.pallas_call(
        body, name="conv_module_bwd", grid=(nt,),
        in_specs=[seq(v0), seq(g0), pl.BlockSpec((CONV_WIDTH, LANES), lambda i: (0, i)), seq(0), ANY],
        out_specs=[ANY, pl.BlockSpec((CONV_PAD, LANES), lambda i: (0, i)), pl.BlockSpec((1, LANES), lambda i: (0, i))],
        out_shape=[jax.ShapeDtypeStruct(dproj.shape, BF16),
                   jax.ShapeDtypeStruct((CONV_PAD, C), F32), jax.ShapeDtypeStruct((1, C), F32)],
        input_output_aliases={4: 0},
        scratch_shapes=[pltpu.VMEM((S + CONV_PAD, LANES), F32), pltpu.VMEM((TIME_BLOCK + CONV_PAD, LANES), F32),
                        pltpu.VMEM((CONV_PAD, LANES), F32),
                        pltpu.VMEM((2, 2, S, LANES), BF16), pltpu.SemaphoreType.DMA((2, 2))],
        compiler_params=_params(("arbitrary",)),
    )(proj, proj, w, dc1, dproj)


def _ffn_fwd_call(u, w, b):
    S, C2 = u.shape
    C = C2 // 2
    width = _pick(C, (2 * LANES, LANES))
    nt = C // width

    def body(ug_ref, uv_ref, wg_ref, wv_ref, bg_ref, bv_ref, f_ref, xg_ref, xv_ref):
        zeros = jnp.zeros((FFN_PAD, LANES), F32)
        for part in range(width // LANES):
            lanes = slice(part * LANES, (part + 1) * LANES)
            xg_ref[0:FFN_PAD, :] = zeros
            xv_ref[0:FFN_PAD, :] = zeros
            xg_ref[FFN_PAD:, :] = ug_ref[:, lanes]
            xv_ref[FFN_PAD:, :] = uv_ref[:, lanes]

            def step(t0, lanes=lanes):
                cg = _conv_block(xg_ref, t0, wg_ref.at[:, lanes], FFN_CONV_WIDTH, FFN_PAD) + bg_ref[:, lanes]
                cv = _conv_block(xv_ref, t0, wv_ref.at[:, lanes], FFN_CONV_WIDTH, FFN_PAD) + bv_ref[:, lanes]
                f_ref[pl.ds(t0, TIME_BLOCK), lanes] = (_gelu(cg) * cv).astype(BF16)

            _time_loop(S, step)

    seq = lambda off: pl.BlockSpec((S, width), lambda i: (0, off + i))
    wsp = lambda off: pl.BlockSpec((FFN_CONV_WIDTH, width), lambda i: (0, off + i))
    bsp = lambda off: pl.BlockSpec((1, width), lambda i: (0, off + i))
    return pl.pallas_call(
        body, name="ffn_conv_geglu", grid=(nt,),
        in_specs=[seq(0), seq(nt), wsp(0), wsp(nt), bsp(0), bsp(nt)],
        out_specs=seq(0), out_shape=jax.ShapeDtypeStruct((S, C), BF16),
        scratch_shapes=[pltpu.VMEM((FFN_PAD + S, LANES), F32)] * 2,
        compiler_params=_params(("parallel",)),
    )(u, u, w, w, b, b)


def _ffn_bwd_call(u, w, b, df):
    S, C2 = u.shape
    C = C2 // 2
    nt = C // LANES

    def body(ug_ref, uv_ref, wg_ref, wv_ref, bg_ref, bv_ref, df_ref,
             du_ref, dwg_ref, dwv_ref, dbg_ref, dbv_ref,
             hg_ref, hv_ref, dg_ref, dv_ref, dwg_acc, dwv_acc, dbg_acc, dbv_acc):
        zeros = jnp.zeros((FFN_PAD, LANES), F32)
        _fill_head(hg_ref, ug_ref, FFN_PAD)
        _fill_head(hv_ref, uv_ref, FFN_PAD)
        dg_ref[S:, :] = zeros
        dv_ref[S:, :] = zeros
        dwg_acc[...] = jnp.zeros_like(dwg_acc)
        dwv_acc[...] = jnp.zeros_like(dwv_acc)
        dbg_acc[...] = jnp.zeros_like(dbg_acc)
        dbv_acc[...] = jnp.zeros_like(dbv_acc)

        def first(t0, xg_ref, xv_ref, x_t0, pad):
            rows = pl.ds(t0, TIME_BLOCK)
            cg = _conv_block(xg_ref, x_t0, wg_ref, FFN_CONV_WIDTH, pad) + bg_ref[...]
            cv = _conv_block(xv_ref, x_t0, wv_ref, FFN_CONV_WIDTH, pad) + bv_ref[...]
            dfb = df_ref[rows, :]
            gelu, gelu_grad = _gelu_and_grad(cg)
            dcg = dfb * cv * gelu_grad
            dcv = dfb * gelu
            dg_ref[rows, :] = dcg
            dv_ref[rows, :] = dcv
            _conv_weight_grad(xg_ref, x_t0, dcg, dwg_acc, FFN_CONV_WIDTH, pad)
            _conv_weight_grad(xv_ref, x_t0, dcv, dwv_acc, FFN_CONV_WIDTH, pad)
            dbg_acc[...] += jnp.sum(dcg, axis=0, keepdims=True)
            dbv_acc[...] += jnp.sum(dcv, axis=0, keepdims=True)

        def second(t0):
            rows = pl.ds(t0, TIME_BLOCK)
            du_ref[0, rows, :] = _conv_transpose_block(dg_ref, t0, wg_ref, FFN_CONV_WIDTH).astype(BF16)
            du_ref[1, rows, :] = _conv_transpose_block(dv_ref, t0, wv_ref, FFN_CONV_WIDTH).astype(BF16)

        first(0, hg_ref, hv_ref, 0, FFN_PAD)
        _time_loop(S, lambda t0: first(t0, ug_ref, uv_ref, t0, 0), skip_first=1)
        _time_loop(S, second)
        dwg_ref[...] = dwg_acc[...]
        dwv_ref[...] = dwv_acc[...]
        dbg_ref[...] = dbg_acc[...]
        dbv_ref[...] = dbv_acc[...]

    seq = lambda off: pl.BlockSpec((S, LANES), lambda i: (0, off + i))
    wsp = lambda off: pl.BlockSpec((FFN_CONV_WIDTH, LANES), lambda i: (0, off + i))
    bsp = lambda off: pl.BlockSpec((1, LANES), lambda i: (0, off + i))
    return pl.pallas_call(
        body, name="ffn_conv_geglu_bwd", grid=(nt,),
        in_specs=[seq(0), seq(nt), wsp(0), wsp(nt), bsp(0), bsp(nt), seq(0)],
        out_specs=[pl.BlockSpec((2, S, LANES), lambda i: (0, 0, i)),
                   pl.BlockSpec((SUBLANES, LANES), lambda i: (0, i)), pl.BlockSpec((SUBLANES, LANES), lambda i: (0, i)),
                   bsp(0), bsp(0)],
        out_shape=[jax.ShapeDtypeStruct((2, S, C), BF16)] + [jax.ShapeDtypeStruct((SUBLANES, C), F32)] * 2
        + [jax.ShapeDtypeStruct((1, C), F32)] * 2,
        scratch_shapes=[pltpu.VMEM((FFN_PAD + TIME_BLOCK, LANES), F32)] * 2 + [pltpu.VMEM((S + FFN_PAD, LANES), F32)] * 2
        + [pltpu.VMEM((SUBLANES, LANES), F32)] * 2
        + [pltpu.VMEM((1, LANES), F32)] * 2,
        compiler_params=_params(("parallel",)),
    )(u, u, w, w, b, b, df)


def _adamw(w_ref, g_ref, m_ref, v_ref, d_ref, mo_ref, vo_ref):
    gv = g_ref[...]
    mn = ADAM_B1 * m_ref[...] + (1.0 - ADAM_B1) * gv
    vn = ADAM_B2 * v_ref[...] + (1.0 - ADAM_B2) * (gv * gv)
    mo_ref[...] = mn
    vo_ref[...] = vn
    m_hat = mn * (1.0 / (1.0 - ADAM_B1 ** ADAM_STEP))
    v_hat = vn * (1.0 / (1.0 - ADAM_B2 ** ADAM_STEP))
    d_ref[...] = -ADAM_LR * (m_hat / (jnp.sqrt(v_hat) + ADAM_EPS) + ADAM_WD * w_ref[...])


def _adamw_call(w, g, m, v, name):
    R, C = w.shape
    tr = _row_tile(R, C)

    def body(w_ref, g_ref, m_ref, v_ref, go_ref, d_ref, mo_ref, vo_ref):
        go_ref[...] = g_ref[...]
        _adamw(w_ref, g_ref, m_ref, v_ref, d_ref, mo_ref, vo_ref)

    spec = pl.BlockSpec((tr, C), lambda i: (i, 0))
    return pl.pallas_call(
        body, name=name, grid=(R // tr,),
        in_specs=[spec] * 4, out_specs=[spec] * 4,
        out_shape=[jax.ShapeDtypeStruct((R, C), F32)] * 4,
        compiler_params=_params(("parallel",)),
    )(w, g, m, v)


def _adamw_small_call(ws, gs, ms, vs):
    n = len(ws)

    def body(*refs):
        w_refs, g_refs, m_refs, v_refs, d_refs, mo_refs, vo_refs = (refs[i * n:(i + 1) * n] for i in range(7))
        for i in range(n):
            _adamw(w_refs[i], g_refs[i], m_refs[i], v_refs[i], d_refs[i], mo_refs[i], vo_refs[i])

    whole = pl.BlockSpec(memory_space=pltpu.VMEM)
    outs = pl.pallas_call(
        body, name="adamw_small",
        in_specs=[whole] * (4 * n), out_specs=[whole] * (3 * n),
        out_shape=[jax.ShapeDtypeStruct(w.shape, F32) for w in ws] * 3,
    )(*ws, *gs, *ms, *vs)
    return outs[:n], outs[n:2 * n], outs[2 * n:]


def _position():
    return lax.axis_index("x"), lax.axis_index("y"), lax.axis_index("c")


def _chip_peers(x, y):
    return [(x, 1 - y), (1 - x, y), (1 - x, 1 - y)]


def _half_rows(ref, core, rows):
    h = rows // 2
    start = pl.multiple_of(core * h, PACKED_ROWS)
    return ref.at[pl.ds(start, h), :] if len(ref.shape) == 2 else ref.at[:, pl.ds(start, h), :]


def _shard_half(ref, shard, core, rows):
    h = rows // 2
    return ref.at[shard, pl.ds(pl.multiple_of(core * h, PACKED_ROWS), h), :]


ANY = pl.BlockSpec(memory_space=pl.ANY)


def _first_hop_copies(srcs, lands):
    x, y, c = _position()
    chip = 2 * x + y
    targets = [(px, py, c) for px, py in _chip_peers(x, y)] + [(x, y, 1 - c)]
    rows = srcs[0].shape[0]
    out = []
    for i, (s, l) in enumerate(zip(srcs, lands)):
        for k, dev in enumerate(targets):
            if i == 0 and k < 3:
                out.append((_half_rows(s, c, rows), _shard_half(l, chip, c, rows), dev, k))
            else:
                out.append((s, l.at[chip], dev, len(targets) * i + k))
    return out


def _second_hop_copies(srcs, lands):
    x, y, c = _position()
    rows = lands[0].shape[1]
    out = []
    for k, (px, py) in enumerate(_chip_peers(x, y)):
        half = _shard_half(lands[0], 2 * px + py, c, rows)
        out.append((half, half, (x, y, 1 - c), k))
    return out


HBM_SPEC = pl.BlockSpec(memory_space=pltpu.HBM)
SEM_SPEC = pl.BlockSpec(memory_space=pltpu.SEMAPHORE)
DATAFLOW = pltpu.SideEffectType.DATAFLOW_SIDE_EFFECTING


def _in_hbm(a):
    return pltpu.with_memory_space_constraint(a, pltpu.HBM)


SIBLING_HANDSHAKE_IDS = dict(grad_exchange_start_w_up=1, grad_assemble_start_others=2, grad_assemble_start_w_in=3)


def _split_start(name, groups, after, carry=None, sibling_only=False):
    spans, arrays = [], []
    for srcs, lands, _, _ in groups:
        spans.append((len(arrays), len(srcs), len(lands)))
        arrays += list(srcs) + list(lands)
    if carry is not None:
        arrays.append(carry)
    na, ng = len(arrays), len(groups)

    def body(*refs):
        sems, token = refs[na + 1:na + 1 + 2 * ng], refs[-1]
        if sibling_only:
            x, y, c = _position()
            barrier = pltpu.get_barrier_semaphore()
            pl.semaphore_signal(barrier, inc=1, device_id=(x, y, 1 - c), device_id_type=MESH)
            pl.semaphore_wait(barrier, 1)
        for g, (_, _, _, copies) in enumerate(groups):
            off, ns, nl = spans[g]
            for src, dst, dev, idx in copies(refs[off:off + ns], refs[off + ns:off + ns + nl]):
                pltpu.make_async_remote_copy(src_ref=src, dst_ref=dst, send_sem=sems[2 * g].at[idx], recv_sem=sems[2 * g + 1].at[idx],
                                             device_id=dev, device_id_type=MESH).start()
        token[...] = jnp.zeros_like(token)

    outs = pl.pallas_call(
        body, name=name,
        in_specs=[HBM_SPEC] * na + [ANY],
        out_specs=[SEM_SPEC] * (2 * ng) + [HBM_SPEC] * na + [pl.BlockSpec(memory_space=pltpu.VMEM)],
        out_shape=[pltpu.SemaphoreType.DMA((n_sems,)) for _, _, n_sems, _ in groups for _ in range(2)]
        + [pltpu.HBM(a.shape, a.dtype) for a in arrays] + [jax.ShapeDtypeStruct((SUBLANES, LANES), F32)],
        input_output_aliases={i: 2 * ng + i for i in range(na)},
        compiler_params=pltpu.CompilerParams(has_side_effects=DATAFLOW,
                                             collective_id=SIBLING_HANDSHAKE_IDS[name] if sibling_only else None),
    )(*[_in_hbm(a) for a in arrays], after)
    started = []
    for g, (off, ns, nl) in enumerate(spans):
        thru = outs[2 * ng + off:2 * ng + off + ns + nl]
        started.append(dict(send=outs[2 * g], recv=outs[2 * g + 1], srcs=list(thru[:ns]), lands=list(thru[ns:]),
                            tile=outs[-1], token=outs[-1][0, 0], carry=None if carry is None else outs[2 * ng + na - 1]))
    return started


def _split_wait(name, started, copies, after):
    n, m = len(started["srcs"]), len(started["lands"])
    after = list(after) if isinstance(after, (list, tuple)) else [after]

    def body(*refs):
        src_refs, land_refs = refs[:n], refs[n:n + m]
        send_sem, recv_sem = refs[n + m], refs[n + m + 1]
        for src, dst, dev, idx in copies(src_refs, land_refs):
            cp = pltpu.make_async_remote_copy(src_ref=src, dst_ref=dst, send_sem=send_sem.at[idx], recv_sem=recv_sem.at[idx],
                                              device_id=dev, device_id_type=MESH)
            cp.wait_send()
            cp.wait_recv()

    arrays = started["srcs"] + started["lands"]
    outs = pl.pallas_call(
        body, name=name,
        in_specs=[HBM_SPEC] * (n + m) + [SEM_SPEC, SEM_SPEC] + [ANY] * len(after),
        out_specs=[HBM_SPEC] * (n + m),
        out_shape=[pltpu.HBM(a.shape, a.dtype) for a in arrays],
        input_output_aliases={i: i for i in range(n + m)},
        compiler_params=pltpu.CompilerParams(has_side_effects=DATAFLOW),
    )(*arrays, started["send"], started["recv"], *after)
    return list(outs)


def _gather_copies(srcs, lands):
    x, y, c = _position()
    chip = 2 * x + y
    targets = [(px, py, c) for px, py in _chip_peers(x, y)] + [(x, y, 1 - c)]
    return [(s, l.at[chip], dev, len(targets) * i + k) for i, (s, l) in enumerate(zip(srcs, lands)) for k, dev in enumerate(targets)]


def _sibling_copies(srcs, lands):
    x, y, c = _position()
    return [(_half_rows(srcs[0], 1 - c, srcs[0].shape[1]), lands[0], (x, y, 1 - c), 0)]


def _sibling_whole_copies(srcs, lands):
    x, y, c = _position()
    return [(srcs[0], lands[0], (x, y, 1 - c), 0)]


def _exchange_copies(srcs, lands):
    x, y, c = _position()
    return [(srcs[0].at[2 * px + py], lands[0].at[k], (px, py, c), k) for k, (px, py) in enumerate(_chip_peers(x, y))]


def _pair_sum_call(grad, recv, chip_core, name):
    _, h, B = recv.shape
    tr = _row_tile(h, B)

    def body(cc_ref, g_ref, r_ref, o_ref, ob_ref):
        s = g_ref[...] + r_ref[...]
        ob_ref[...] = s.astype(BF16)

        @pl.when(pl.program_id(1) == cc_ref[0])
        def _():
            o_ref[...] = s

    g_spec = pl.BlockSpec((None, tr, B), lambda i, q, cc_ref: (q, cc_ref[1] * (h // tr) + i, 0))
    spec = pl.BlockSpec((None, tr, B), lambda i, q, cc_ref: (q, i, 0))
    own_spec = pl.BlockSpec((tr, B), lambda i, q, cc_ref: (i, 0))
    return pl.pallas_call(
        body, name=name,
        grid_spec=pltpu.PrefetchScalarGridSpec(num_scalar_prefetch=1, grid=(h // tr, N_CHIPS), in_specs=[g_spec, spec],
                                               out_specs=[own_spec, spec]),
        out_shape=[jax.ShapeDtypeStruct((h, B), F32), jax.ShapeDtypeStruct(recv.shape, BF16)],
        compiler_params=_params(("parallel", "arbitrary")),
    )(chip_core, grad, recv)


def _chip_sum_call(partial, recv, chip_core, name):
    _, h, B = recv.shape
    tr = _row_tile(h, B)

    def body(cc_ref, p_ref, r_ref, o_ref):
        o_ref[...] = ((p_ref[...] + r_ref[0].astype(F32)) + r_ref[1].astype(F32)) + r_ref[2].astype(F32)

    return pl.pallas_call(
        body, name=name,
        grid_spec=pltpu.PrefetchScalarGridSpec(
            num_scalar_prefetch=1, grid=(h // tr,),
            in_specs=[pl.BlockSpec((tr, B), lambda i, cc_ref: (i, 0)),
                      pl.BlockSpec((3, tr, B), lambda i, cc_ref: (0, i, 0))],
            out_specs=pl.BlockSpec((tr, B), lambda i, cc_ref: (cc_ref[1] * (h // tr) + i, 0))),
        out_shape=jax.ShapeDtypeStruct((2 * h, B), F32),
        compiler_params=_params(("parallel",)),
    )(chip_core, partial, recv)


def _assemble_copies(srcs, lands):
    x, y, c = _position()
    out = []
    for i, land in enumerate(lands):
        half = _half_rows(land, c, land.shape[0])
        out.append((half, half, (x, y, 1 - c), i))
    return out


N_DEVICES = 8


def _allsum_copies(srcs, lands):
    x, y, c = _position()
    me = 4 * x + 2 * y + c
    out = []
    for k in range(1, N_DEVICES):
        peer = (1 - x if k & 4 else x, 1 - y if k & 2 else y, 1 - c if k & 1 else c)
        out.append((srcs[0], lands[0].at[me], peer, k - 1))
    return out


def _ordered_sum_call(mine, landed, me_chip, shapes, sharded_cols):
    rows = mine.shape[0]
    outs = [(s[0], n) if n else s for s, n in zip(shapes, sharded_cols)]

    def body(mc_ref, x_ref, l_ref, *refs):
        acc_ref = refs[-1]
        acc = jnp.where(mc_ref[0] == 0, x_ref[...], l_ref[0])
        for d in range(1, N_DEVICES):
            acc = acc + jnp.where(mc_ref[0] == d, x_ref[...], l_ref[d])
        acc_ref[...] = acc
        first = 0
        for o_ref, (r, c), n in zip(refs[:-1], shapes, sharded_cols):
            per_row = c // LANES

            def unpack(chip, o_ref=o_ref, r=r, n=n, per_row=per_row, first=first):
                for i in range(r):
                    for j in range((n or per_row * LANES) // LANES):
                        src = first + i * per_row + chip * ((n or 0) // LANES) + j
                        o_ref[i:i + 1, j * LANES:(j + 1) * LANES] = acc_ref[src:src + 1, :]

            if n:
                for q in range(N_CHIPS):
                    pl.when(mc_ref[1] == q)(functools.partial(unpack, q))
            else:
                unpack(0)
            first += r * per_row

    results = pl.pallas_call(
        body, name="small_grad_sum",
        in_specs=[pl.BlockSpec(memory_space=pltpu.SMEM), pl.BlockSpec(memory_space=pltpu.VMEM), pl.BlockSpec(memory_space=pltpu.VMEM)],
        out_specs=[pl.BlockSpec(memory_space=pltpu.VMEM)] * len(outs),
        out_shape=[jax.ShapeDtypeStruct(s, F32) for s in outs],
        scratch_shapes=[pltpu.VMEM((rows, LANES), F32)],
    )(me_chip, mine, landed)
    return results


def _pack(arrays):
    flat = jnp.concatenate([a.reshape(-1).astype(F32) for a in arrays])
    rows = -(-flat.shape[0] // LANES)
    rows = -(-rows // SUBLANES) * SUBLANES
    flat = jnp.pad(flat, (0, rows * LANES - flat.shape[0]))
    return flat.reshape(rows, LANES)


def _local_step(xs, target, P, late_weights, on_grad):
    S, D = xs.shape
    qkv_width = 3 * N_HEADS * HEAD_DIM
    glu_col0, gate_col0 = qkv_width, qkv_width + 2 * D
    shard_major = lambda g: g.reshape(N_CHIPS, g.shape[0] // N_CHIPS, g.shape[1])

    h1 = _rms_fwd_call(xs, P["norm_mix_pre"])
    buckets = _bucket_tables()
    bias = _bias_table_call(P["rel_bias"] + 0.0 * h1[0, 0].astype(F32), buckets)
    P = dict(P, **late_weights("in", bias))
    proj = _matmul(h1, P["w_in"], "nn", "proj_in")
    parts = []
    for g in range(N_GROUPS):
        parts += _attn_fwd_call(proj, bias, g)
    a, a_bf, lse = _attn_merge_call(parts)
    P = dict(P, **late_weights("mix", a_bf))
    y_a = _matmul(a_bf, P["w_attn_out"], "nn", "attn_out")
    c1 = _conv_fwd_call(proj, glu_col0, P["conv_dw_w"], P["conv_dw_b"])
    cact = _ln_silu_call(c1, P["conv_ln_g"], P["conv_ln_b"])
    y_c = _matmul(cact, P["conv_pw_w"], "nn", "conv_pw")
    mixed = _mix_call(proj, gate_col0, P["b_gate"], y_a, y_c)
    out = _matmul(mixed, P["w_out"], "nn", "mix_out")
    x1, h2 = _res1_call(xs, out, P["norm_mix_post"], P["norm_ffn_pre"])
    P = dict(P, **late_weights("up", h2))
    u = _matmul(h2, P["w_up"], "nn", "ffn_up")
    f = _ffn_fwd_call(u, P["ffn_conv_w"], P["ffn_conv_b"])
    P = dict(P, **late_weights("down", f))
    yff = _matmul(f, P["w_down"], "nn", "ffn_down")
    loss_tile, dx2, dyff, dg_ffn_post = _loss_call(yff, x1, P["norm_ffn_post"], target)

    G = {}
    G["norm_ffn_post"] = dg_ffn_post
    on_grad("w_down", shard_major(_matmul(f, dyff, "tn", "ffn_down_dw")))
    df = _matmul(dyff, P["w_down"], "nt", "ffn_down_dx")
    du, dwg, dwv, dbg, dbv = _ffn_bwd_call(u, P["ffn_conv_w"], P["ffn_conv_b"], df)
    G["ffn_conv_w"] = jnp.concatenate([dwg[:FFN_CONV_WIDTH], dwv[:FFN_CONV_WIDTH]], axis=1)
    G["ffn_conv_b"] = jnp.concatenate([dbg, dbv], axis=1)
    du = on_grad("w_up", functools.partial(_grad_half_matmul, h2, "ffn_up_dw"), carry=du)
    dh2 = _matmul(du, P["w_up"], "nt", "ffn_up_dx")
    dx1, dout, G["norm_ffn_pre"], G["norm_mix_post"] = _mid_bwd_call(x1, P["norm_ffn_pre"], dh2, dx2, out, P["norm_mix_post"])
    on_grad("w_out", shard_major(_matmul(mixed, dout, "tn", "mix_out_dw")))
    dmixed = _matmul(dout, P["w_out"], "nt", "mix_out_dx")
    dya, dyc, dproj, dba, dbc = _mix_bwd_call(dmixed, proj, gate_col0, P["b_gate"], y_a, y_c)
    G["b_gate"] = jnp.concatenate([dba, dbc], axis=1)
    on_grad("w_attn_out", _matmul(a_bf, dya, "tn", "attn_out_dw", out_shards=True))
    dyc = on_grad("conv_pw_w", shard_major(_matmul(cact, dyc, "tn", "conv_pw_dw")), carry=dyc)
    da = _matmul(dya, P["w_attn_out"], "nt", "attn_out_dx")
    dcact = _matmul(dyc, P["conv_pw_w"], "nt", "conv_pw_dx")
    dc1, G["conv_ln_g"], G["conv_ln_b"] = _ln_silu_bwd_call(c1, P["conv_ln_g"], P["conv_ln_b"], dcact)
    dproj, dw_dw, G["conv_dw_b"] = _conv_bwd_call(proj, glu_col0, P["conv_dw_w"], dc1, dproj)
    G["conv_dw_w"] = dw_dw[:CONV_WIDTH]
    delta = _attn_delta_call(a, da)
    dbs = []
    for g in range(N_GROUPS):
        dproj, db = _attn_bwd_call(proj, bias, da, lse, delta, g, dproj)
        dbs.append(db)
    G["rel_bias"] = _bias_grad_call(jnp.concatenate(dbs, axis=0), buckets)
    dproj = on_grad("w_in", functools.partial(_grad_half_matmul, h1, "proj_in_dw"), carry=dproj)
    dh1 = _matmul(dproj, P["w_in"], "nt", "proj_in_dx")
    dh1 = on_grad(None, None, carry=dh1)
    grad_x, G["norm_mix_pre"] = _in_bwd_call(xs, P["norm_mix_pre"], dh1, dx1)
    return loss_tile, grad_x, G


def kernel(x, w_in, b_gate, rel_bias, w_attn_out, conv_dw_w, conv_dw_b, conv_ln_g, conv_ln_b, conv_pw_w, w_out, norm_mix_pre, norm_mix_post, norm_ffn_pre, norm_ffn_post, w_up, ffn_conv_w, ffn_conv_b, w_down, loss_target, m_w_in, m_b_gate, m_rel_bias, m_w_attn_out, m_conv_dw_w, m_conv_dw_b, m_conv_ln_g, m_conv_ln_b, m_conv_pw_w, m_w_out, m_norm_mix_pre, m_norm_mix_post, m_norm_ffn_pre, m_norm_ffn_post, m_w_up, m_ffn_conv_w, m_ffn_conv_b, m_w_down, v_w_in, v_b_gate, v_rel_bias, v_w_attn_out, v_conv_dw_w, v_conv_dw_b, v_conv_ln_g, v_conv_ln_b, v_conv_pw_w, v_w_out, v_norm_mix_pre, v_norm_mix_post, v_norm_ffn_pre, v_norm_ffn_post, v_w_up, v_ffn_conv_w, v_ffn_conv_b, v_w_down):
    weights = dict(w_in=w_in, b_gate=b_gate, rel_bias=rel_bias, w_attn_out=w_attn_out, conv_dw_w=conv_dw_w, conv_dw_b=conv_dw_b,
                   conv_ln_g=conv_ln_g, conv_ln_b=conv_ln_b, conv_pw_w=conv_pw_w, w_out=w_out, norm_mix_pre=norm_mix_pre,
                   norm_mix_post=norm_mix_post, norm_ffn_pre=norm_ffn_pre, norm_ffn_post=norm_ffn_post, w_up=w_up,
                   ffn_conv_w=ffn_conv_w, ffn_conv_b=ffn_conv_b, w_down=w_down)
    m_in = dict(w_in=m_w_in, b_gate=m_b_gate, rel_bias=m_rel_bias, w_attn_out=m_w_attn_out, conv_dw_w=m_conv_dw_w,
                conv_dw_b=m_conv_dw_b, conv_ln_g=m_conv_ln_g, conv_ln_b=m_conv_ln_b, conv_pw_w=m_conv_pw_w, w_out=m_w_out,
                norm_mix_pre=m_norm_mix_pre, norm_mix_post=m_norm_mix_post, norm_ffn_pre=m_norm_ffn_pre,
                norm_ffn_post=m_norm_ffn_post, w_up=m_w_up, ffn_conv_w=m_ffn_conv_w, ffn_conv_b=m_ffn_conv_b, w_down=m_w_down)
    v_in = dict(w_in=v_w_in, b_gate=v_b_gate, rel_bias=v_rel_bias, w_attn_out=v_w_attn_out, conv_dw_w=v_conv_dw_w,
                conv_dw_b=v_conv_dw_b, conv_ln_g=v_conv_ln_g, conv_ln_b=v_conv_ln_b, conv_pw_w=v_conv_pw_w, w_out=v_w_out,
                norm_mix_pre=v_norm_mix_pre, norm_mix_post=v_norm_mix_post, norm_ffn_pre=v_norm_ffn_pre,
                norm_ffn_post=v_norm_ffn_post, w_up=v_w_up, ffn_conv_w=v_ffn_conv_w, ffn_conv_b=v_ffn_conv_b, w_down=v_w_down)
    names = list(weights)
    xi, yi, ci = _position()
    chip = 2 * xi + yi
    core_arr = jnp.reshape(ci, (1,)).astype(jnp.int32)

    xs = x[0]
    target = loss_target[0]
    S, D = xs.shape

    big = ["w_in", "w_attn_out", "conv_pw_w", "w_out", "w_up", "w_down"]
    row_sharded = ("conv_pw_w", "w_out", "w_down")
    natural = lambda k, g: g.reshape(-1, g.shape[2]) if k in row_sharded else g
    first_srcs = [w_in[0].astype(BF16), conv_dw_w[0], ffn_conv_w[0]]
    first_lands = [lax.empty((N_CHIPS,) + s.shape, s.dtype) for s in first_srcs]
    (first_hop,) = _split_start("gather_in_start", [(first_srcs, first_lands, 4 * len(first_srcs), _first_hop_copies)], core_arr)
    launched = first_hop["token"]
    late_sets = dict(mix=["w_attn_out", "conv_pw_w", "w_out"], up=["w_up"], down=["w_down"])
    late_groups = []
    for keys in late_sets.values():
        srcs = [(weights[k][0] + launched).astype(BF16) for k in keys]
        late_groups.append((srcs, [lax.empty((N_CHIPS,) + s.shape, BF16) for s in srcs], 4 * len(keys), _gather_copies))
    started = {}

    def late_weights(tag, after):
        if tag == "in":
            casts = [s for srcs, _, _, _ in late_groups for s in srcs]
            w_in_halves, dw4, fc4 = _split_wait("gather_in_wait", first_hop, _first_hop_copies, [after] + casts)[len(first_srcs):]
            second_hop, *late = _split_start("gather_in_pass_start", [([], [w_in_halves], 3, _second_hop_copies)] + late_groups, dw4)
            started.update(zip(late_sets, late))
            (w_in_full,) = _split_wait("gather_in_pass_wait", second_hop, _second_hop_copies, second_hop["tile"])
            return dict(w_in=w_in_full, conv_dw_w=jnp.concatenate(list(dw4), axis=1), ffn_conv_w=jnp.concatenate(list(fc4), axis=1))
        landed = _split_wait(f"gather_{tag}_wait", started[tag], _gather_copies, after)[len(late_sets[tag]):]
        return {k: natural(k, g) for k, g in zip(late_sets[tag], landed)}

    chip_core = jnp.stack([chip, ci]).astype(jnp.int32)
    exchanging, pending, second_half = {}, {}, {}

    held = []

    def launch(tag, after, carry=None):
        keys, groups, partial = [], [], {}
        for k in list(exchanging):
            st, copies = exchanging.pop(k)
            gk, r1 = _split_wait(f"sibling_exchange_wait_{k}", st, copies, after)
            if k in second_half:
                partial[k], s16 = second_half.pop(k)(init=r1)
            else:
                partial[k], s16 = _pair_sum_call(gk, r1, chip_core, f"pair_sum_{k}")
            keys.append(k)
            groups.append(([s16], [lax.empty((3,) + s16.shape[1:], BF16)], 3, _exchange_copies))
        fresh = [(k, copies) for k, _, copies in held]
        for _, g3, copies in held:
            rows = g3.shape[1] // 2 if copies is _sibling_copies else g3.shape[1]
            groups.append(([g3], [lax.empty((N_CHIPS, rows, g3.shape[2]), F32)], 1, copies))
        held.clear()
        begun = _split_start(f"grad_exchange_start_{tag}", groups, core_arr, carry, sibling_only=not keys)
        for k, st in zip(keys, begun):
            pending[k] = (partial[k], st)
        for (k, copies), st in zip(fresh, begun[len(keys):]):
            exchanging[k] = (st, copies)
        return begun[0]["carry"]

    others = [k for k in big if k != "w_in"]
    assembling = {}

    def on_grad(k, g, carry=None):
        if k is None:
            carried = launch("last", carry[:SUBLANES, :LANES], carry)
            assembling["others"] = assemble_start(others, carried, "others", carried)
            return assembling["others"]["carry"]
        if callable(g):
            theirs = g(jnp.stack([chip, 1 - ci]).astype(jnp.int32), carry)
            held.append((k, theirs, _sibling_whole_copies))
            carried = launch(k, theirs[0, :SUBLANES, :LANES], carry)
            second_half[k] = functools.partial(g, chip_core, carried)
            return carried
        held.append((k, g, _sibling_copies))
        if k in ("w_down", "w_out", "w_attn_out"):
            return carry
        return launch(k, g[0, :SUBLANES, :LANES], carry)

    def assemble_start(keys, after, tag, carry=None):
        halves = []
        for k in keys:
            s32, st = pending[k]
            recv2 = _split_wait(f"chip_exchange_wait_{k}", st, _exchange_copies, after)[1]
            halves.append(_chip_sum_call(s32, recv2, chip_core, f"chip_sum_{k}"))
        (st,) = _split_start(f"grad_assemble_start_{tag}", [([], halves, len(halves), _assemble_copies)], core_arr, carry,
                             sibling_only=True)
        return st

    def assemble_wait(keys, st, after, tag):
        return dict(zip(keys, _split_wait(f"grad_assemble_wait_{tag}", st, _assemble_copies, after)))

    P = dict(b_gate=b_gate, rel_bias=rel_bias, conv_dw_b=conv_dw_b, conv_ln_g=conv_ln_g, conv_ln_b=conv_ln_b,
             norm_mix_pre=norm_mix_pre + launched, norm_mix_post=norm_mix_post, norm_ffn_pre=norm_ffn_pre,
             norm_ffn_post=norm_ffn_post, ffn_conv_b=ffn_conv_b)
    loss_tile, grad_x, G = _local_step(xs, target, P, late_weights, on_grad)

    small = [k for k in names if k not in big]
    packed = _pack([loss_tile[:1]] + [G[k] for k in small])
    (allsum,) = _split_start("small_grad_allsum_start",
                             [([packed], [jnp.zeros((N_DEVICES,) + packed.shape, F32)], N_DEVICES - 1, _allsum_copies)], core_arr)

    reduced, grads, deltas, new_m, new_v = {}, {}, {}, {}, {}

    def update(keys):
        for k in keys:
            gk, d, mn, vn = _adamw_call(weights[k][0], reduced[k], m_in[k][0], v_in[k][0], f"adamw_{k}")
            grads[k], deltas[k], new_m[k], new_v[k] = gk[None], d[None], mn[None], vn[None]

    reduced.update(assemble_wait(others, assembling["others"], [allsum["tile"], grad_x], "others"))
    update(others)
    assembling["w_in"] = assemble_start(["w_in"], [deltas[k] for k in others], "w_in")

    me_chip = jnp.stack([4 * xi + 2 * yi + ci, chip]).astype(jnp.int32)
    mine, landed = _split_wait("small_grad_allsum_wait", allsum, _allsum_copies, assembling["w_in"]["tile"])
    piece_shapes = [(1, LANES)] + [(G[k].size // LANES, LANES) if k == "rel_bias" else G[k].shape for k in small]
    piece_cols = [0] + [weights[k].shape[2] if k in ("conv_dw_w", "ffn_conv_w") else 0 for k in small]
    loss_row, *summed = _ordered_sum_call(mine, landed, me_chip, piece_shapes, piece_cols)
    loss = loss_row[0, 0]
    for k, gsum in zip(small, summed):
        grads[k] = gsum.reshape(weights[k].shape)
    ds, mns, vns = _adamw_small_call([weights[k] for k in small], [grads[k] for k in small],
                                     [m_in[k] for k in small], [v_in[k] for k in small])
    deltas.update(zip(small, ds))
    new_m.update(zip(small, mns))
    new_v.update(zip(small, vns))
    reduced.update(assemble_wait(["w_in"], assembling["w_in"], list(ds), "w_in"))
    update(["w_in"])

    return (loss, grad_x[None], *[grads[k] for k in names], *[deltas[k] for k in names],
            *[new_m[k] for k in names], *[new_v[k] for k in names])
```

```python
import functools
import math

import jax
import jax.numpy as jnp
import numpy as np
from jax import lax
from jax.experimental import pallas as pl
from jax.experimental.pallas import tpu as pltpu

F32 = jnp.float32
BF16 = jnp.bfloat16
MESH = pl.DeviceIdType.MESH

HEAD_DIM = 128
HEADS_PER_GROUP = 4
DILATED_PATTERNS = ((128, 1), (512, 4), (2048, 16))
N_GROUPS = 3
N_HEADS = N_GROUPS * HEADS_PER_GROUP
SPAN = 128
GROUP_WIDTH = HEADS_PER_GROUP * HEAD_DIM
CONV_WIDTH = 31
FFN_CONV_WIDTH = 3
N_BUCKETS = 32
MAX_DISTANCE = 2048
RMS_EPS = 1e-6
LN_EPS = 1e-5
NEG_INF = -1e30
ADAM_LR = 0.001
ADAM_B1 = 0.9
ADAM_B2 = 0.999
ADAM_EPS = 1e-08
ADAM_WD = 0.01
ADAM_STEP = 10

LANES = 128
SUBLANES = 8
PACKED_ROWS = 16
ROW_TILE = 512
GATE_ROWS, GATE_COLS = 512, 512
TIME_BLOCK = 128
CONV_PAD = 32
FFN_PAD = 8
VMEM_LIMIT = 56 << 20


def _params(sem=None, vmem=None):
    kw = {}
    if sem is not None:
        kw["dimension_semantics"] = sem
    if vmem is not None:
        kw["vmem_limit_bytes"] = vmem
    return pltpu.CompilerParams(**kw)


def _pick(n, cands):
    for c in cands:
        if n % c == 0:
            return c
    return n


ELEMENTWISE_TILE_BYTES = 3 << 19


def _row_tile(rows, cols):
    for align in (16, SUBLANES):
        fits = [t for t in range(align, rows + 1, align) if rows % t == 0 and t * cols * 4 <= ELEMENTWISE_TILE_BYTES]
        if fits:
            return max(fits)
    return SUBLANES


N_CHIPS = 4
M_TILES = (1024, 1408, 512, 256, 128)
N_TILES = (1024, 512, 1408, 256, 128)
K_TILES = (2176, 2048, 1408, 1024, 512, 256, 128)


def _matmul(a, b, mode, name, out_shards=False, tm=None):
    assert a.dtype == BF16 and b.dtype == BF16, (name, a.dtype, b.dtype)
    b3 = b.ndim == 3
    tn = tk = None
    halves = None
    if mode == "nn":
        M, K = a.shape
        N = b.shape[-1] * (N_CHIPS if b3 else 1)
        tn = b.shape[-1] if b3 else None
    elif mode == "nt":
        if a.ndim == 3:
            halves = a.shape[2]
        M, K = a.shape[-2], a.shape[-1] * (a.shape[0] if a.ndim == 3 else 1)
        N = b.shape[-2]
        tk = b.shape[-1] if b3 else None
    else:
        if b3:
            halves = b.shape[2]
        K, M = a.shape
        N = b.shape[-1] * (b.shape[0] if b3 else 1)
        tn = N // N_CHIPS if out_shards else None
    tm = tm or _pick(M, M_TILES)
    tn = tn or _pick(N, N_TILES)
    tk = tk or _pick(K, K_TILES)
    nk = K // tk
    dn = {"nn": (((1,), (0,)), ((), ())), "nt": (((1,), (1,)), ((), ())), "tn": (((0,), (0,)), ((), ()))}[mode]

    def body(a_ref, b_ref, o_ref):
        if nk == 1:
            o_ref[...] = lax.dot_general(a_ref[...], b_ref[...], dn, preferred_element_type=F32)
        else:
            @pl.when(pl.program_id(2) == 0)
            def _():
                o_ref[...] = jnp.zeros_like(o_ref)

            o_ref[...] += lax.dot_general(a_ref[...], b_ref[...], dn, preferred_element_type=F32)

    if mode == "tn":
        a_spec = pl.BlockSpec((tk, tm), lambda i, j, k: (k, i))
    elif halves:
        per = halves // tk
        a_spec = pl.BlockSpec((None, tm, tk), lambda i, j, k: (k // per, i, k % per))
    else:
        a_spec = pl.BlockSpec((tm, tk), lambda i, j, k: (i, k))
    if mode == "nn":
        b_spec = pl.BlockSpec((None, tk, tn), lambda i, j, k: (j, k, 0)) if b3 else pl.BlockSpec((tk, tn), lambda i, j, k: (k, j))
    elif mode == "nt":
        b_spec = pl.BlockSpec((None, tn, tk), lambda i, j, k: (k, j, 0)) if b3 else pl.BlockSpec((tn, tk), lambda i, j, k: (j, k))
    elif halves:
        per = halves // tn
        b_spec = pl.BlockSpec((None, tk, tn), lambda i, j, k: (j // per, k, j % per))
    else:
        b_spec = pl.BlockSpec((tk, tn), lambda i, j, k: (k, j))
    if out_shards:
        out_spec = pl.BlockSpec((None, tm, tn), lambda i, j, k: (j, i, 0))
        out_shape = jax.ShapeDtypeStruct((N_CHIPS, M, tn), F32)
    else:
        out_spec = pl.BlockSpec((tm, tn), lambda i, j, k: (i, j))
        out_shape = jax.ShapeDtypeStruct((M, N), F32)
    return pl.pallas_call(
        body, name=name, grid=(M // tm, N // tn, nk),
        in_specs=[a_spec, b_spec], out_specs=out_spec, out_shape=out_shape,
        compiler_params=_params(("parallel", "parallel", "arbitrary"), VMEM_LIMIT),
    )(a, b)


def _grad_half_matmul(a, name, chip_half, b, init=None):
    K, M = a.shape
    parts = b.ndim == 3
    N = b.shape[-1] * (b.shape[0] if parts else 1)
    h, tn = M // 2, N // N_CHIPS
    summed = init is not None
    dn = (((0,), (0,)), ((), ()))

    def body(ch_ref, a_ref, b_ref, *rest):
        product = lax.dot_general(a_ref[...], b_ref[...], dn, preferred_element_type=F32)
        if not summed:
            rest[0][...] = product
            return
        init_ref, own_ref, sum16_ref = rest
        total = product + init_ref[...]
        sum16_ref[...] = total.astype(BF16)

        @pl.when(pl.program_id(0) == ch_ref[0])
        def _():
            own_ref[...] = total

    if parts:
        per = b.shape[2] // tn
        b_spec = pl.BlockSpec((None, K, tn), lambda j, ch_ref: (j // per, 0, j % per))
    else:
        b_spec = pl.BlockSpec((K, tn), lambda j, ch_ref: (0, j))
    shard_spec = pl.BlockSpec((None, h, tn), lambda j, ch_ref: (j, 0, 0))
    own_spec = pl.BlockSpec((h, tn), lambda j, ch_ref: (0, 0))
    shape = (N_CHIPS, h, tn)
    return pl.pallas_call(
        body, name=name + ("_mine" if summed else "_theirs"),
        grid_spec=pltpu.PrefetchScalarGridSpec(
            num_scalar_prefetch=1, grid=(N_CHIPS,),
            in_specs=[pl.BlockSpec((K, h), lambda j, ch_ref: (0, ch_ref[1])), b_spec] + [shard_spec] * summed,
            out_specs=[own_spec, shard_spec] if summed else shard_spec),
        out_shape=[jax.ShapeDtypeStruct((h, tn), F32), jax.ShapeDtypeStruct(shape, BF16)] if summed
        else jax.ShapeDtypeStruct(shape, F32),
        compiler_params=_params(("arbitrary",), VMEM_LIMIT),
    )(chip_half, a, b, *([init] if summed else []))


def _rms(x, g):
    r = lax.rsqrt(jnp.mean(x * x, axis=-1, keepdims=True) + RMS_EPS)
    return x * r * g


def _rms_bwd(x, g, dy):
    r = lax.rsqrt(jnp.mean(x * x, axis=-1, keepdims=True) + RMS_EPS)
    n = x * r
    dn = dy * g
    dx = r * (dn - n * jnp.mean(dn * n, axis=-1, keepdims=True))
    return dx, jnp.sum(dy * n, axis=0, keepdims=True)


def _sigmoid(x):
    return 1.0 / (1.0 + jnp.exp(-x))


_GELU_C = math.sqrt(2.0 / math.pi)


def _gelu(x):
    return 0.5 * x * (1.0 + jnp.tanh(_GELU_C * (x + 0.044715 * x * x * x)))


def _gelu_and_grad(x):
    x2 = x * x
    t = jnp.tanh(_GELU_C * x * (1.0 + 0.044715 * x2))
    half = 0.5 * (1.0 + t)
    return x * half, half + (0.5 * _GELU_C) * x * (1.0 - t * t) * (1.0 + (3.0 * 0.044715) * x2)


def _row_spec(width, col_block=0):
    return pl.BlockSpec((ROW_TILE, width), lambda i: (i, col_block))


def _vec_spec(width, col_block=0):
    return pl.BlockSpec((1, width), lambda i: (0, col_block))


def _accumulate(ref, part):
    @pl.when(pl.program_id(0) == 0)
    def _():
        ref[...] = part

    @pl.when(pl.program_id(0) > 0)
    def _():
        ref[...] += part


def _rms_fwd_call(x, g):
    S, D = x.shape

    def body(x_ref, g_ref, h_ref):
        h_ref[...] = _rms(x_ref[...], g_ref[...]).astype(BF16)

    return pl.pallas_call(
        body, name="rms_mix_pre", grid=(S // ROW_TILE,),
        in_specs=[_row_spec(D), _vec_spec(D)], out_specs=_row_spec(D),
        out_shape=jax.ShapeDtypeStruct((S, D), BF16),
        compiler_params=_params(("parallel",)),
    )(x, g)


def _ln_silu_call(c1, g, b):
    S, C = c1.shape

    def body(c_ref, g_ref, b_ref, o_ref):
        xv = c_ref[...]
        mu = jnp.mean(xv, axis=-1, keepdims=True)
        xc = xv - mu
        var = jnp.mean(xc * xc, axis=-1, keepdims=True)
        z = xc * lax.rsqrt(var + LN_EPS) * g_ref[...] + b_ref[...]
        o_ref[...] = (z * _sigmoid(z)).astype(BF16)

    return pl.pallas_call(
        body, name="conv_ln_silu", grid=(S // ROW_TILE,),
        in_specs=[_row_spec(C), _vec_spec(C), _vec_spec(C)], out_specs=_row_spec(C),
        out_shape=jax.ShapeDtypeStruct((S, C), BF16),
        compiler_params=_params(("parallel",)),
    )(c1, g, b)


def _ln_silu_bwd_call(c1, g, b, dc):
    S, C = c1.shape

    def body(c_ref, g_ref, b_ref, dc_ref, dx_ref, dg_ref, db_ref):
        xv = c_ref[...]
        mu = jnp.mean(xv, axis=-1, keepdims=True)
        xc = xv - mu
        rs = lax.rsqrt(jnp.mean(xc * xc, axis=-1, keepdims=True) + LN_EPS)
        xh = xc * rs
        z = xh * g_ref[...] + b_ref[...]
        sg = _sigmoid(z)
        dz = dc_ref[...] * (sg * (1.0 + z * (1.0 - sg)))
        dxh = dz * g_ref[...]
        dx_ref[...] = rs * (dxh - jnp.mean(dxh, axis=-1, keepdims=True) - xh * jnp.mean(dxh * xh, axis=-1, keepdims=True))
        _accumulate(dg_ref, jnp.sum(dz * xh, axis=0, keepdims=True))
        _accumulate(db_ref, jnp.sum(dz, axis=0, keepdims=True))

    return pl.pallas_call(
        body, name="conv_ln_silu_bwd", grid=(S // ROW_TILE,),
        in_specs=[_row_spec(C), _vec_spec(C), _vec_spec(C), _row_spec(C)],
        out_specs=[_row_spec(C), _vec_spec(C), _vec_spec(C)],
        out_shape=[jax.ShapeDtypeStruct((S, C), F32), jax.ShapeDtypeStruct((1, C), F32), jax.ShapeDtypeStruct((1, C), F32)],
        compiler_params=_params(("arbitrary",)),
    )(c1, g, b, dc)


def _mix_call(proj, gate_col0, b_gate, y_a, y_c):
    S, D = y_a.shape
    w = GATE_COLS
    nc = D // w
    ga0, gc0 = gate_col0 // w, (gate_col0 + D) // w

    def body(ga_ref, gc_ref, ba_ref, bc_ref, ya_ref, yc_ref, o_ref):
        o_ref[...] = (_sigmoid(ga_ref[...] + ba_ref[...]) * ya_ref[...]
                      + _sigmoid(gc_ref[...] + bc_ref[...]) * yc_ref[...]).astype(BF16)

    tile = lambda off: pl.BlockSpec((GATE_ROWS, w), lambda i, j: (i, off + j))
    vec = lambda off: pl.BlockSpec((1, w), lambda i, j: (0, off + j))
    return pl.pallas_call(
        body, name="gate_mix", grid=(S // GATE_ROWS, nc),
        in_specs=[tile(ga0), tile(gc0), vec(0), vec(nc), tile(0), tile(0)],
        out_specs=tile(0), out_shape=jax.ShapeDtypeStruct((S, D), BF16),
        compiler_params=_params(("parallel", "parallel")),
    )(proj, proj, b_gate, b_gate, y_a, y_c)


def _window_stores(stage_ref, slot, dst_ref, rows, cols, sems):
    width = stage_ref.shape[-1]
    return [pltpu.make_async_copy(stage_ref.at[slot, p], dst_ref.at[rows, pl.ds(pl.multiple_of(c, LANES), width)], sems.at[slot, p])
            for p, c in enumerate(cols)]


def _staged_window_stores(stage_ref, dst_ref, sems, step, n_steps, rows, cols, fill):
    slot = step % 2
    copies = lambda s: _window_stores(stage_ref, s, dst_ref, rows, cols, sems)

    @pl.when(step >= 2)
    def _():
        for cp in copies(slot):
            cp.wait()

    fill(slot)
    for cp in copies(slot):
        cp.start()

    @pl.when(step == n_steps - 1)
    def _():
        for s in ([slot, 1 - slot] if n_steps > 1 else [slot]):
            for cp in copies(s):
                cp.wait()


def _mix_bwd_call(dmixed, proj, gate_col0, b_gate, y_a, y_c):
    S, D = y_a.shape
    w = GATE_COLS
    nc = D // w
    nr = S // GATE_ROWS
    ga0, gc0 = gate_col0 // w, (gate_col0 + D) // w

    def body(dm_ref, ga_ref, gc_ref, ba_ref, bc_ref, ya_ref, yc_ref, dya_ref, dyc_ref, dproj_ref, dba_ref, dbc_ref,
             stage_ref, sems):
        j, i = pl.program_id(0), pl.program_id(1)
        dm = dm_ref[...]
        sa = _sigmoid(ga_ref[...] + ba_ref[...])
        sc = _sigmoid(gc_ref[...] + bc_ref[...])
        dya_ref[...] = (dm * sa).astype(BF16)
        dyc_ref[...] = (dm * sc).astype(BF16)
        dga = dm * ya_ref[...] * sa * (1.0 - sa)
        dgc = dm * yc_ref[...] * sc * (1.0 - sc)

        def fill(slot):
            stage_ref[slot, 0] = dga.astype(BF16)
            stage_ref[slot, 1] = dgc.astype(BF16)

        rows = pl.ds(pl.multiple_of(i * GATE_ROWS, GATE_ROWS), GATE_ROWS)
        _staged_window_stores(stage_ref, dproj_ref, sems, j * nr + i, nc * nr, rows,
                              [gate_col0 + j * w, gate_col0 + D + j * w], fill)
        pa = jnp.sum(dga, axis=0, keepdims=True)
        pc = jnp.sum(dgc, axis=0, keepdims=True)

        @pl.when(i == 0)
        def _():
            dba_ref[...] = pa
            dbc_ref[...] = pc

        @pl.when(i > 0)
        def _():
            dba_ref[...] += pa
            dbc_ref[...] += pc

    tile = lambda off: pl.BlockSpec((GATE_ROWS, w), lambda j, i: (i, off + j))
    vec = lambda off: pl.BlockSpec((1, w), lambda j, i: (0, off + j))
    return pl.pallas_call(
        body, name="gate_mix_bwd", grid=(nc, nr),
        in_specs=[tile(0), tile(ga0), tile(gc0), vec(0), vec(nc), tile(0), tile(0)],
        out_specs=[tile(0), tile(0), ANY, vec(0), vec(0)],
        out_shape=[jax.ShapeDtypeStruct((S, D), BF16)] * 2 + [jax.ShapeDtypeStruct((S, proj.shape[1]), BF16)] + [
                   jax.ShapeDtypeStruct((1, D), F32), jax.ShapeDtypeStruct((1, D), F32)],
        scratch_shapes=[pltpu.VMEM((2, 2, GATE_ROWS, w), BF16), pltpu.SemaphoreType.DMA((2, 2))],
        compiler_params=_params(("arbitrary", "arbitrary")),
    )(dmixed, proj, proj, b_gate, b_gate, y_a, y_c)


def _res1_call(x, out, g_post, g_pre):
    S, D = x.shape

    def body(x_ref, o_ref, gp_ref, gq_ref, x1_ref, h2_ref):
        x1 = x_ref[...] + _rms(o_ref[...], gp_ref[...])
        x1_ref[...] = x1
        h2_ref[...] = _rms(x1, gq_ref[...]).astype(BF16)

    return pl.pallas_call(
        body, name="residual_mix", grid=(S // ROW_TILE,),
        in_specs=[_row_spec(D), _row_spec(D), _vec_spec(D), _vec_spec(D)],
        out_specs=[_row_spec(D), _row_spec(D)],
        out_shape=[jax.ShapeDtypeStruct((S, D), F32), jax.ShapeDtypeStruct((S, D), BF16)],
        compiler_params=_params(("parallel",)),
    )(x, out, g_post, g_pre)


def _loss_call(y, x1, g_post, target):
    S, D = y.shape

    def body(y_ref, x1_ref, g_ref, t_ref, loss_ref, dx_ref, dy_ref, dg_ref):
        yv, gv = y_ref[...], g_ref[...]
        err = x1_ref[...] + _rms(yv, gv) - t_ref[...]
        dx2 = err * (1.0 / D)
        dx_ref[...] = dx2
        dy, dg = _rms_bwd(yv, gv, dx2)
        dy_ref[...] = dy.astype(BF16)
        _accumulate(dg_ref, dg)
        part = 0.5 * jnp.sum(jnp.mean(err * err, axis=-1, keepdims=True), axis=0, keepdims=True)
        _accumulate(loss_ref, jnp.broadcast_to(part, (SUBLANES, LANES)))

    return pl.pallas_call(
        body, name="residual_ffn_loss", grid=(S // ROW_TILE,),
        in_specs=[_row_spec(D), _row_spec(D), _vec_spec(D), _row_spec(D)],
        out_specs=[pl.BlockSpec((SUBLANES, LANES), lambda i: (0, 0)), _row_spec(D), _row_spec(D), _vec_spec(D)],
        out_shape=[jax.ShapeDtypeStruct((SUBLANES, LANES), F32), jax.ShapeDtypeStruct((S, D), F32),
                   jax.ShapeDtypeStruct((S, D), BF16), jax.ShapeDtypeStruct((1, D), F32)],
        compiler_params=_params(("arbitrary",)),
    )(y, x1, g_post, target)


def _mid_bwd_call(x1, g_pre, dh2, dx2, out, g_post):
    S, D = x1.shape

    def body(x1_ref, gq_ref, dh_ref, dx2_ref, o_ref, gp_ref, dx1_ref, do_ref, dgq_ref, dgp_ref):
        d, dgq = _rms_bwd(x1_ref[...], gq_ref[...], dh_ref[...])
        dx1 = dx2_ref[...] + d
        dx1_ref[...] = dx1
        do, dgp = _rms_bwd(o_ref[...], gp_ref[...], dx1)
        do_ref[...] = do.astype(BF16)
        _accumulate(dgq_ref, dgq)
        _accumulate(dgp_ref, dgp)

    return pl.pallas_call(
        body, name="residual_mix_bwd", grid=(S // ROW_TILE,),
        in_specs=[_row_spec(D), _vec_spec(D), _row_spec(D), _row_spec(D), _row_spec(D), _vec_spec(D)],
        out_specs=[_row_spec(D), _row_spec(D), _vec_spec(D), _vec_spec(D)],
        out_shape=[jax.ShapeDtypeStruct((S, D), F32), jax.ShapeDtypeStruct((S, D), BF16)] + [jax.ShapeDtypeStruct((1, D), F32)] * 2,
        compiler_params=_params(("arbitrary",)),
    )(x1, g_pre, dh2, dx2, out, g_post)


def _in_bwd_call(x, g, dh1, dx1):
    S, D = x.shape

    def body(x_ref, g_ref, dh_ref, dx1_ref, gx_ref, dg_ref):
        d, dg = _rms_bwd(x_ref[...], g_ref[...], dh_ref[...])
        gx_ref[...] = dx1_ref[...] + d
        _accumulate(dg_ref, dg)

    return pl.pallas_call(
        body, name="rms_mix_pre_bwd", grid=(S // ROW_TILE,),
        in_specs=[_row_spec(D), _vec_spec(D), _row_spec(D), _row_spec(D)],
        out_specs=[_row_spec(D), _vec_spec(D)],
        out_shape=[jax.ShapeDtypeStruct((S, D), F32), jax.ShapeDtypeStruct((1, D), F32)],
        compiler_params=_params(("arbitrary",)),
    )(x, g, dh1, dx1)


def _bucket_table(dilation):
    qi = np.arange(SPAN)[:, None]
    ki = np.arange(2 * SPAN)[None, :]
    dist = np.maximum(qi + SPAN - ki, 0) * dilation
    max_exact = N_BUCKETS // 2
    d = np.maximum(dist, 1).astype(np.float64)
    large = max_exact + (np.log(d / max_exact) / math.log(MAX_DISTANCE / max_exact) * (N_BUCKETS - max_exact)).astype(np.int32)
    large = np.minimum(large, N_BUCKETS - 1)
    return np.where(dist < max_exact, dist, large).astype(np.int32)


def _bucket_tables():
    return jnp.asarray(np.stack([_bucket_table(r) for _, r in DILATED_PATTERNS]))


def _bias_table_call(rel_bias, buckets):
    def body(rb_ref, bk_ref, o_ref):
        for h in range(N_HEADS):
            bk = bk_ref[h // HEADS_PER_GROUP]

            def step(b, acc):
                return jnp.where(bk == b, rb_ref[b, h], acc)

            o_ref[h] = lax.fori_loop(0, N_BUCKETS, step, jnp.zeros((SPAN, 2 * SPAN), F32))

    return pl.pallas_call(
        body, name="rel_bias_table",
        in_specs=[pl.BlockSpec(memory_space=pltpu.SMEM), pl.BlockSpec(memory_space=pltpu.VMEM)],
        out_specs=pl.BlockSpec(memory_space=pltpu.VMEM),
        out_shape=jax.ShapeDtypeStruct((N_HEADS, SPAN, 2 * SPAN), F32),
    )(rel_bias, buckets)


def _bias_grad_call(dbias, buckets):
    def body(db_ref, bk_ref, o_ref, rows_ref):
        for h in range(N_HEADS):
            bk = bk_ref[h // HEADS_PER_GROUP]
            dv = db_ref[h]

            def step(b, carry):
                rows_ref[h, b] = jnp.sum(jnp.where(bk == b, dv, 0.0), axis=0, keepdims=True)
                return carry

            lax.fori_loop(0, N_BUCKETS, step, 0)
        o_ref[...] = jnp.sum(rows_ref[...], axis=-1, keepdims=True)

    out = pl.pallas_call(
        body, name="rel_bias_grad",
        in_specs=[pl.BlockSpec(memory_space=pltpu.VMEM), pl.BlockSpec(memory_space=pltpu.VMEM)],
        out_specs=pl.BlockSpec(memory_space=pltpu.VMEM),
        out_shape=jax.ShapeDtypeStruct((N_HEADS, N_BUCKETS, 1, 1), F32),
        scratch_shapes=[pltpu.VMEM((N_HEADS, N_BUCKETS, 1, 2 * SPAN), F32)],
    )(dbias, buckets)
    return out.reshape(N_HEADS, N_BUCKETS).T


def _dot_nt(a, b):
    return lax.dot_general(a, b, (((1,), (1,)), ((), ())), preferred_element_type=F32)


def _dot_nn(a, b):
    return lax.dot_general(a, b, (((1,), (0,)), ((), ())), preferred_element_type=F32)


def _dot_tn(a, b):
    return lax.dot_general(a, b, (((0,), (0,)), ((), ())), preferred_element_type=F32)


def _band_masks(n, nb):
    qi = lax.broadcasted_iota(jnp.int32, (SPAN, SPAN), 0)
    ki = lax.broadcasted_iota(jnp.int32, (SPAN, SPAN), 1)
    prev_ok = jnp.logical_and(ki >= qi, n > 0)
    cur_ok = ki <= qi
    next_ok = jnp.logical_and(ki >= qi, n < nb - 1)
    return prev_ok, cur_ok, next_ok


def _wide_band_mask(n):
    qi = lax.broadcasted_iota(jnp.int32, (SPAN, 2 * SPAN), 0)
    ki = lax.broadcasted_iota(jnp.int32, (SPAN, 2 * SPAN), 1)
    prev_ok = jnp.logical_and(jnp.logical_and(ki < SPAN, ki >= qi), n > 0)
    cur_ok = jnp.logical_and(ki >= SPAN, ki - SPAN <= qi)
    return jnp.logical_or(prev_ok, cur_ok)


def _attn_plan(S, group):
    r = DILATED_PATTERNS[group][1]
    hp, per = (HEADS_PER_GROUP, 1) if r == 1 else (2, 4)
    return r, S // (r * SPAN), hp, per


def _residue_rows(rho, r):
    return slice(None) if r == 1 else pl.ds(rho, SPAN, stride=r)


def _for_residues(r, per, fn):
    if r == per:
        for u in range(per):
            fn(u)
        return

    def step(i, carry):
        for u in range(per):
            fn(i * per + u)
        return carry

    lax.fori_loop(0, r // per, step, 0)


def _attn_fwd_call(proj, bias, group):
    S = proj.shape[0]
    r, nb, hp, per = _attn_plan(S, group)
    scale = HEAD_DIM ** -0.5
    kinds = ("q", "kp", "kc", "vp", "vc") if nb > 1 else ("q", "kc", "vc")

    per_kind = _refs_per_kind(r, hp)

    def body(*refs):
        ins = {kind: refs[i * per_kind:(i + 1) * per_kind] for i, kind in enumerate(kinds)}
        b_ref, o_ref, lse_ref = refs[len(kinds) * per_kind:]
        n = pl.program_id(1)
        prev_ok, cur_ok, _ = _band_masks(n, nb)

        band_ok = _wide_band_mask(n) if nb > 1 else cur_ok

        def residue(rho):
            rows = _residue_rows(rho, r)
            for j in range(hp):
                get = lambda kind: _head_rows(ins[kind], j, rows, r).astype(BF16)
                q = get("q")
                if nb > 1:
                    keys, vals, bias_j = jnp.concatenate([get("kp"), get("kc")], axis=0), jnp.concatenate([get("vp"), get("vc")], axis=0), b_ref[j]
                else:
                    keys, vals, bias_j = get("kc"), get("vc"), b_ref[j, :, SPAN:]
                s = jnp.where(band_ok, _dot_nt(q, keys) * scale + bias_j, NEG_INF)
                m = jnp.max(s, axis=-1, keepdims=True)
                p = jnp.exp(s - m)
                den = jnp.sum(p, axis=-1, keepdims=True)
                o_ref[j, rows, :] = _dot_nn(p.astype(BF16), vals) / den
                lse_ref[j, rows, :] = jnp.broadcast_to(m + jnp.log(den), (SPAN, HEAD_DIM))

        _for_residues(r, per, residue)

    in_specs = [_head_spec(r, nb, hp, kind, group, jj) for kind in kinds for jj in range(per_kind)]
    in_specs.append(pl.BlockSpec((hp, SPAN, 2 * SPAN), lambda j, n: (group * (HEADS_PER_GROUP // hp) + j, 0, 0)))
    out = pl.BlockSpec((hp, r * SPAN, HEAD_DIM), lambda j, n: (j, n, 0))
    return pl.pallas_call(
        body, name=f"attn_fwd_g{group}", grid=(HEADS_PER_GROUP // hp, nb),
        in_specs=in_specs, out_specs=[out] * 2,
        out_shape=[jax.ShapeDtypeStruct((HEADS_PER_GROUP, S, HEAD_DIM), F32)] * 2,
        compiler_params=_params(("parallel", "parallel"), VMEM_LIMIT),
    )(*([proj] * (len(in_specs) - 1)), bias)


_PROJ_PART = dict(q=0, qn=0, kp=1, kc=1, vp=2, vc=2)


def _refs_per_kind(r, hp):
    return 1 if r == 1 else hp


def _head_rows(refs, j, rows, r):
    return refs[0][:, j * HEAD_DIM:(j + 1) * HEAD_DIM] if r == 1 else refs[j][rows, :]


def _head_spec(r, nb, hp, kind, group, jj):
    if kind in _PROJ_PART:
        base = (_PROJ_PART[kind] * N_GROUPS + group) * HEADS_PER_GROUP
    else:
        base = 0
    if kind.endswith("p"):
        row = lambda n: jnp.maximum(n - 1, 0)
    elif kind.endswith("n"):
        row = lambda n: jnp.minimum(n + 1, nb - 1)
    else:
        row = lambda n: n
    if r == 1:
        return pl.BlockSpec((SPAN, hp * HEAD_DIM), lambda j, n: (row(n), base // hp + j))
    return pl.BlockSpec((r * SPAN, HEAD_DIM), lambda j, n: (row(n), base + j * hp + jj))


def _attn_merge_call(parts):
    S = parts[0].shape[1]

    def body(o1, s1, o2, s2, o3, s3, a_ref, ab_ref, lse_ref):
        for j in range(HEADS_PER_GROUP):
            sl = slice(j * HEAD_DIM, (j + 1) * HEAD_DIM)
            mx = jnp.maximum(jnp.maximum(s1[j], s2[j]), s3[j])
            w1 = jnp.exp(s1[j] - mx)
            w2 = jnp.exp(s2[j] - mx)
            w3 = jnp.exp(s3[j] - mx)
            den = w1 + w2 + w3
            a = (w1 * o1[j] + w2 * o2[j] + w3 * o3[j]) / den
            a_ref[:, sl] = a
            ab_ref[:, sl] = a.astype(BF16)
            lse_ref[:, sl] = mx + jnp.log(den)

    heads = pl.BlockSpec((HEADS_PER_GROUP, ROW_TILE, HEAD_DIM), lambda i: (0, i, 0))
    return pl.pallas_call(
        body, name="attn_merge", grid=(S // ROW_TILE,),
        in_specs=[heads] * 6, out_specs=[_row_spec(GROUP_WIDTH)] * 3,
        out_shape=[jax.ShapeDtypeStruct((S, GROUP_WIDTH), F32), jax.ShapeDtypeStruct((S, GROUP_WIDTH), BF16),
                   jax.ShapeDtypeStruct((S, GROUP_WIDTH), F32)],
        compiler_params=_params(("parallel",)),
    )(*parts)


def _attn_bwd_call(proj, bias, a, da, lse, group, dproj):
    S = proj.shape[0]
    r, nb, hp, per = _attn_plan(S, group)
    scale = HEAD_DIM ** -0.5
    kinds = ("q", "qn", "kp", "kc", "vp", "vc", "da", "dan", "lse", "lsen", "a", "an") if nb > 1 else ("q", "kc", "vc", "da", "lse", "a")
    source = dict(da=da, dan=da, lse=lse, lsen=lse, a=a, an=a)

    per_kind = _refs_per_kind(r, hp)
    per_group = HEADS_PER_GROUP // hp
    block_rows = r * SPAN

    def body(*refs):
        ins = {kind: refs[i * per_kind:(i + 1) * per_kind] for i, kind in enumerate(kinds)}
        b_ref, _, dproj_ref, db_ref, stage_ref, sems = refs[len(kinds) * per_kind:][:6]
        strided_ref = None if r == 1 else refs[-1]
        jg, n = pl.program_id(0), pl.program_id(1)
        prev_ok, cur_ok, next_ok = _band_masks(n, nb)

        @pl.when(n == 0)
        def _():
            db_ref[...] = jnp.zeros_like(db_ref)

        band_ok = _wide_band_mask(n) if nb > 1 else cur_ok

        def fill(slot):
            def put(part, j, rows, value):
                if r == 1:
                    stage_ref[slot, part, :, j * HEAD_DIM:(j + 1) * HEAD_DIM] = value.astype(BF16)
                else:
                    strided_ref[part, j, rows, :] = value

            _for_residues(r, per, functools.partial(residue, put))
            if r > 1:
                for part in range(3):
                    for j in range(hp):
                        for t0 in range(0, block_rows, ROW_TILE):
                            stage_ref[slot, part, t0:t0 + ROW_TILE, j * HEAD_DIM:(j + 1) * HEAD_DIM] = (
                                strided_ref[part, j, t0:t0 + ROW_TILE, :].astype(BF16))

        def residue(put, rho):
            rows = _residue_rows(rho, r)
            for j in range(hp):
                get = lambda kind: _head_rows(ins[kind], j, rows, r)
                q = get("q").astype(BF16)
                kc = get("kc").astype(BF16)
                vc = get("vc").astype(BF16)
                da_q = get("da")
                dav = da_q.astype(BF16)
                lse_q = get("lse")
                dl_q = jnp.sum(get("a") * da_q, axis=-1, keepdims=True)
                if nb == 1:
                    pc = jnp.exp(jnp.where(cur_ok, _dot_nt(q, kc) * scale + b_ref[j, :, SPAN:], NEG_INF) - lse_q)
                    dsc = pc * (_dot_nt(dav, vc) - dl_q)
                    dsc_b = dsc.astype(BF16)
                    dq = _dot_nn(dsc_b, kc)
                    dk = _dot_tn(dsc_b, q)
                    dv = _dot_tn(pc.astype(BF16), dav)
                    db_ref[j, :, SPAN:] += dsc
                else:
                    qn = get("qn").astype(BF16)
                    da_n = get("dan")
                    dan = da_n.astype(BF16)
                    keys = jnp.concatenate([get("kp").astype(BF16), kc], axis=0)
                    vals = jnp.concatenate([get("vp").astype(BF16), vc], axis=0)
                    wide = lambda t: jnp.concatenate([t, t], axis=1)
                    p = jnp.exp(jnp.where(band_ok, _dot_nt(q, keys) * scale + b_ref[j], NEG_INF) - wide(lse_q))
                    ds = p * (_dot_nt(dav, vals) - dl_q)
                    dq = _dot_nn(ds.astype(BF16), keys)
                    db_ref[j] += ds
                    pn = jnp.exp(jnp.where(next_ok, _dot_nt(qn, kc) * scale + b_ref[j, :, :SPAN], NEG_INF) - get("lsen"))
                    dsn = pn * (_dot_nt(dan, vc) - jnp.sum(get("an") * da_n, axis=-1, keepdims=True))
                    both = lambda cur_part, next_part: jnp.concatenate([cur_part.astype(BF16), next_part.astype(BF16)], axis=0)
                    dk = _dot_tn(both(ds[:, SPAN:], dsn), jnp.concatenate([q, qn], axis=0))
                    dv = _dot_tn(both(p[:, SPAN:], pn), jnp.concatenate([dav, dan], axis=0))
                put(0, j, rows, dq * scale)
                put(1, j, rows, dk * scale)
                put(2, j, rows, dv)

        cols = [(part * N_GROUPS + group) * GROUP_WIDTH + jg * (hp * HEAD_DIM) for part in range(3)]
        rows = pl.ds(pl.multiple_of(n * block_rows, SPAN), block_rows)
        _staged_window_stores(stage_ref, dproj_ref, sems, jg * nb + n, per_group * nb, rows, cols, fill)

    band = (hp, SPAN, 2 * SPAN)
    in_specs = [_head_spec(r, nb, hp, kind, group, jj) for kind in kinds for jj in range(per_kind)]
    in_specs += [pl.BlockSpec(band, lambda j, n: (group * per_group + j, 0, 0)), ANY]
    operands = [source.get(kind, proj) for kind in kinds for _ in range(per_kind)] + [bias, dproj]
    scratch = [pltpu.VMEM((2, 3, block_rows, hp * HEAD_DIM), BF16), pltpu.SemaphoreType.DMA((2, 3))]
    if r > 1:
        scratch.append(pltpu.VMEM((3, hp, block_rows, HEAD_DIM), F32))
    return pl.pallas_call(
        body, name=f"attn_bwd_g{group}", grid=(per_group, nb),
        in_specs=in_specs,
        out_specs=[ANY, pl.BlockSpec(band, lambda j, n: (j, 0, 0))],
        out_shape=[jax.ShapeDtypeStruct(dproj.shape, BF16), jax.ShapeDtypeStruct((HEADS_PER_GROUP, SPAN, 2 * SPAN), F32)],
        input_output_aliases={len(operands) - 1: 0},
        scratch_shapes=scratch,
        compiler_params=_params(("arbitrary", "arbitrary"), VMEM_LIMIT),
    )(*operands)


def _tap_rows(xpad_ref, t0, k, width, pad):
    return xpad_ref[pl.ds(t0 + (pad - (width - 1 - k)), TIME_BLOCK), :]


def _conv_block(xpad_ref, t0, w_ref, width, pad):
    acc = None
    for k in range(width):
        term = w_ref[k:k + 1, :] * _tap_rows(xpad_ref, t0, k, width, pad)
        acc = term if acc is None else acc + term
    return acc


def _conv_transpose_block(dpad_ref, t0, w_ref, width):
    acc = None
    for k in range(width):
        term = w_ref[k:k + 1, :] * dpad_ref[pl.ds(t0 + (width - 1 - k), TIME_BLOCK), :]
        acc = term if acc is None else acc + term
    return acc


def _conv_weight_grad(xpad_ref, t0, dy, dw_ref, width, pad):
    for k in range(width):
        dw_ref[k:k + 1, :] += jnp.sum(dy * _tap_rows(xpad_ref, t0, k, width, pad), axis=0, keepdims=True)


def _time_loop(S, step, skip_first=0, skip_last=0):
    def it(tb, carry):
        step(pl.multiple_of(tb * TIME_BLOCK, TIME_BLOCK))
        return carry

    lax.fori_loop(skip_first, S // TIME_BLOCK - skip_last, it, 0)


def _fill_head(head_ref, x_ref, pad):
    head_ref[0:pad, :] = jnp.zeros((pad, LANES), F32)
    head_ref[pad:, :] = x_ref[0:TIME_BLOCK, :]


def _fill_tail(tail_ref, x_ref, pad):
    S = x_ref.shape[0]
    tail_ref[0:TIME_BLOCK, :] = x_ref[S - TIME_BLOCK:S, :]
    tail_ref[TIME_BLOCK:, :] = jnp.zeros((pad, LANES), F32)


def _conv_fwd_call(proj, col0, w, b):
    S = proj.shape[0]
    C = w.shape[1]
    nt = C // LANES
    v0, g0 = col0 // LANES, (col0 + C) // LANES

    def body(val_ref, gate_ref, w_ref, b_ref, o_ref, pad_ref):
        pad_ref[0:CONV_PAD, :] = jnp.zeros((CONV_PAD, LANES), F32)
        pad_ref[CONV_PAD:, :] = val_ref[...] * _sigmoid(gate_ref[...])

        def step(t0):
            o_ref[pl.ds(t0, TIME_BLOCK), :] = _conv_block(pad_ref, t0, w_ref, CONV_WIDTH, CONV_PAD) + b_ref[...]

        _time_loop(S, step)

    seq = lambda off: pl.BlockSpec((S, LANES), lambda i: (0, off + i))
    return pl.pallas_call(
        body, name="conv_module", grid=(nt,),
        in_specs=[seq(v0), seq(g0), pl.BlockSpec((CONV_WIDTH, LANES), lambda i: (0, i)), pl.BlockSpec((1, LANES), lambda i: (0, i))],
        out_specs=seq(0), out_shape=jax.ShapeDtypeStruct((S, C), F32),
        scratch_shapes=[pltpu.VMEM((S + CONV_PAD, LANES), F32)],
        compiler_params=_params(("parallel",)),
    )(proj, proj, w, b)


def _conv_bwd_call(proj, col0, w, dc1, dproj):
    S = proj.shape[0]
    C = w.shape[1]
    nt = C // LANES
    v0, g0 = col0 // LANES, (col0 + C) // LANES

    def body(val_ref, gate_ref, w_ref, dy_ref, _, dproj_ref, dw_ref, db_ref, xpad_ref, tail_ref, dwacc_ref, stage_ref, sems):
        i = pl.program_id(0)
        xpad_ref[0:CONV_PAD, :] = jnp.zeros((CONV_PAD, LANES), F32)
        xpad_ref[CONV_PAD:, :] = val_ref[...] * _sigmoid(gate_ref[...])
        _fill_tail(tail_ref, dy_ref, CONV_PAD)
        dwacc_ref[...] = jnp.zeros_like(dwacc_ref)

        def fill(slot):
            def block(t0, dy_src, dy_t0):
                rows = pl.ds(t0, TIME_BLOCK)
                _conv_weight_grad(xpad_ref, t0, dy_ref[rows, :], dwacc_ref, CONV_WIDTH, CONV_PAD)
                dc0 = _conv_transpose_block(dy_src, dy_t0, w_ref, CONV_WIDTH)
                sg = _sigmoid(gate_ref[rows, :])
                stage_ref[slot, 0, rows, :] = (dc0 * sg).astype(BF16)
                stage_ref[slot, 1, rows, :] = (dc0 * val_ref[rows, :] * sg * (1.0 - sg)).astype(BF16)

            _time_loop(S, lambda t0: block(t0, dy_ref, t0), skip_last=1)
            block(S - TIME_BLOCK, tail_ref, 0)

        _staged_window_stores(stage_ref, dproj_ref, sems, i, nt, pl.ds(0, S),
                              [col0 + i * LANES, col0 + C + i * LANES], fill)
        dw_ref[...] = dwacc_ref[...]
        db_ref[...] = jnp.sum(dy_ref[...], axis=0, keepdims=True)

    seq = lambda off: pl.BlockSpec((S, LANES), lambda i: (0, off + i))
    return pl.pallas_call(
        body, name="conv_module_bwd", grid=(nt,),
        in_specs=[seq(v0), seq(g0), pl.BlockSpec((CONV_WIDTH, LANES), lambda i: (0, i)), seq(0), ANY],
        out_specs=[ANY, pl.BlockSpec((CONV_PAD, LANES), lambda i: (0, i)), pl.BlockSpec((1, LANES), lambda i: (0, i))],
        out_shape=[jax.ShapeDtypeStruct(dproj.shape, BF16),
                   jax.ShapeDtypeStruct((CONV_PAD, C), F32), jax.ShapeDtypeStruct((1, C), F32)],
        input_output_aliases={4: 0},
        scratch_shapes=[pltpu.VMEM((S + CONV_PAD, LANES), F32), pltpu.VMEM((TIME_BLOCK + CONV_PAD, LANES), F32),
                        pltpu.VMEM((CONV_PAD, LANES), F32),
                        pltpu.VMEM((2, 2, S, LANES), BF16), pltpu.SemaphoreType.DMA((2, 2))],
        compiler_params=_params(("arbitrary",)),
    )(proj, proj, w, dc1, dproj)


def _ffn_fwd_call(u, w, b):
    S, C2 = u.shape
    C = C2 // 2
    width = _pick(C, (2 * LANES, LANES))
    nt = C // width

    def body(ug_ref, uv_ref, wg_ref, wv_ref, bg_ref, bv_ref, f_ref, xg_ref, xv_ref):
        zeros = jnp.zeros((FFN_PAD, LANES), F32)
        for part in range(width // LANES):
            lanes = slice(part * LANES, (part + 1) * LANES)
            xg_ref[0:FFN_PAD, :] = zeros
            xv_ref[0:FFN_PAD, :] = zeros
            xg_ref[FFN_PAD:, :] = ug_ref[:, lanes]
            xv_ref[FFN_PAD:, :] = uv_ref[:, lanes]

            def step(t0, lanes=lanes):
                cg = _conv_block(xg_ref, t0, wg_ref.at[:, lanes], FFN_CONV_WIDTH, FFN_PAD) + bg_ref[:, lanes]
                cv = _conv_block(xv_ref, t0, wv_ref.at[:, lanes], FFN_CONV_WIDTH, FFN_PAD) + bv_ref[:, lanes]
                f_ref[pl.ds(t0, TIME_BLOCK), lanes] = (_gelu(cg) * cv).astype(BF16)

            _time_loop(S, step)

    seq = lambda off: pl.BlockSpec((S, width), lambda i: (0, off + i))
    wsp = lambda off: pl.BlockSpec((FFN_CONV_WIDTH, width), lambda i: (0, off + i))
    bsp = lambda off: pl.BlockSpec((1, width), lambda i: (0, off + i))
    return pl.pallas_call(
        body, name="ffn_conv_geglu", grid=(nt,),
        in_specs=[seq(0), seq(nt), wsp(0), wsp(nt), bsp(0), bsp(nt)],
        out_specs=seq(0), out_shape=jax.ShapeDtypeStruct((S, C), BF16),
        scratch_shapes=[pltpu.VMEM((FFN_PAD + S, LANES), F32)] * 2,
        compiler_params=_params(("parallel",)),
    )(u, u, w, w, b, b)


def _ffn_bwd_call(u, w, b, df):
    S, C2 = u.shape
    C = C2 // 2
    nt = C // LANES

    def body(ug_ref, uv_ref, wg_ref, wv_ref, bg_ref, bv_ref, df_ref,
             du_ref, dwg_ref, dwv_ref, dbg_ref, dbv_ref,
             hg_ref, hv_ref, dg_ref, dv_ref, dwg_acc, dwv_acc, dbg_acc, dbv_acc):
        zeros = jnp.zeros((FFN_PAD, LANES), F32)
        _fill_head(hg_ref, ug_ref, FFN_PAD)
        _fill_head(hv_ref, uv_ref, FFN_PAD)
        dg_ref[S:, :] = zeros
        dv_ref[S:, :] = zeros
        dwg_acc[...] = jnp.zeros_like(dwg_acc)
        dwv_acc[...] = jnp.zeros_like(dwv_acc)
        dbg_acc[...] = jnp.zeros_like(dbg_acc)
        dbv_acc[...] = jnp.zeros_like(dbv_acc)

        def first(t0, xg_ref, xv_ref, x_t0, pad):
            rows = pl.ds(t0, TIME_BLOCK)
            cg = _conv_block(xg_ref, x_t0, wg_ref, FFN_CONV_WIDTH, pad) + bg_ref[...]
            cv = _conv_block(xv_ref, x_t0, wv_ref, FFN_CONV_WIDTH, pad) + bv_ref[...]
            dfb = df_ref[rows, :]
            gelu, gelu_grad = _gelu_and_grad(cg)
            dcg = dfb * cv * gelu_grad
            dcv = dfb * gelu
            dg_ref[rows, :] = dcg
            dv_ref[rows, :] = dcv
            _conv_weight_grad(xg_ref, x_t0, dcg, dwg_acc, FFN_CONV_WIDTH, pad)
            _conv_weight_grad(xv_ref, x_t0, dcv, dwv_acc, FFN_CONV_WIDTH, pad)
            dbg_acc[...] += jnp.sum(dcg, axis=0, keepdims=True)
            dbv_acc[...] += jnp.sum(dcv, axis=0, keepdims=True)

        def second(t0):
            rows = pl.ds(t0, TIME_BLOCK)
            du_ref[0, rows, :] = _conv_transpose_block(dg_ref, t0, wg_ref, FFN_CONV_WIDTH).astype(BF16)
            du_ref[1, rows, :] = _conv_transpose_block(dv_ref, t0, wv_ref, FFN_CONV_WIDTH).astype(BF16)

        first(0, hg_ref, hv_ref, 0, FFN_PAD)
        _time_loop(S, lambda t0: first(t0, ug_ref, uv_ref, t0, 0), skip_first=1)
        _time_loop(S, second)
        dwg_ref[...] = dwg_acc[...]
        dwv_ref[...] = dwv_acc[...]
        dbg_ref[...] = dbg_acc[...]
        dbv_ref[...] = dbv_acc[...]

    seq = lambda off: pl.BlockSpec((S, LANES), lambda i: (0, off + i))
    wsp = lambda off: pl.BlockSpec((FFN_CONV_WIDTH, LANES), lambda i: (0, off + i))
    bsp = lambda off: pl.BlockSpec((1, LANES), lambda i: (0, off + i))
    return pl.pallas_call(
        body, name="ffn_conv_geglu_bwd", grid=(nt,),
        in_specs=[seq(0), seq(nt), wsp(0), wsp(nt), bsp(0), bsp(nt), seq(0)],
        out_specs=[pl.BlockSpec((2, S, LANES), lambda i: (0, 0, i)),
                   pl.BlockSpec((SUBLANES, LANES), lambda i: (0, i)), pl.BlockSpec((SUBLANES, LANES), lambda i: (0, i)),
                   bsp(0), bsp(0)],
        out_shape=[jax.ShapeDtypeStruct((2, S, C), BF16)] + [jax.ShapeDtypeStruct((SUBLANES, C), F32)] * 2
        + [jax.ShapeDtypeStruct((1, C), F32)] * 2,
        scratch_shapes=[pltpu.VMEM((FFN_PAD + TIME_BLOCK, LANES), F32)] * 2 + [pltpu.VMEM((S + FFN_PAD, LANES), F32)] * 2
        + [pltpu.VMEM((SUBLANES, LANES), F32)] * 2
        + [pltpu.VMEM((1, LANES), F32)] * 2,
        compiler_params=_params(("parallel",)),
    )(u, u, w, w, b, b, df)


def _adamw(w_ref, g_ref, m_ref, v_ref, d_ref, mo_ref, vo_ref):
    gv = g_ref[...]
    mn = ADAM_B1 * m_ref[...] + (1.0 - ADAM_B1) * gv
    vn = ADAM_B2 * v_ref[...] + (1.0 - ADAM_B2) * (gv * gv)
    mo_ref[...] = mn
    vo_ref[...] = vn
    m_hat = mn * (1.0 / (1.0 - ADAM_B1 ** ADAM_STEP))
    v_hat = vn * (1.0 / (1.0 - ADAM_B2 ** ADAM_STEP))
    d_ref[...] = -ADAM_LR * (m_hat / (jnp.sqrt(v_hat) + ADAM_EPS) + ADAM_WD * w_ref[...])


def _adamw_call(w, g, m, v, name):
    R, C = w.shape
    tr = _row_tile(R, C)

    def body(w_ref, g_ref, m_ref, v_ref, go_ref, d_ref, mo_ref, vo_ref):
        go_ref[...] = g_ref[...]
        _adamw(w_ref, g_ref, m_ref, v_ref, d_ref, mo_ref, vo_ref)

    spec = pl.BlockSpec((tr, C), lambda i: (i, 0))
    return pl.pallas_call(
        body, name=name, grid=(R // tr,),
        in_specs=[spec] * 4, out_specs=[spec] * 4,
        out_shape=[jax.ShapeDtypeStruct((R, C), F32)] * 4,
        compiler_params=_params(("parallel",)),
    )(w, g, m, v)


def _adamw_small_call(ws, gs, ms, vs):
    n = len(ws)

    def body(*refs):
        w_refs, g_refs, m_refs, v_refs, d_refs, mo_refs, vo_refs = (refs[i * n:(i + 1) * n] for i in range(7))
        for i in range(n):
            _adamw(w_refs[i], g_refs[i], m_refs[i], v_refs[i], d_refs[i], mo_refs[i], vo_refs[i])

    whole = pl.BlockSpec(memory_space=pltpu.VMEM)
    outs = pl.pallas_call(
        body, name="adamw_small",
        in_specs=[whole] * (4 * n), out_specs=[whole] * (3 * n),
        out_shape=[jax.ShapeDtypeStruct(w.shape, F32) for w in ws] * 3,
    )(*ws, *gs, *ms, *vs)
    return outs[:n], outs[n:2 * n], outs[2 * n:]


def _position():
    return lax.axis_index("x"), lax.axis_index("y"), lax.axis_index("c")


def _chip_peers(x, y):
    return [(x, 1 - y), (1 - x, y), (1 - x, 1 - y)]


def _half_rows(ref, core, rows):
    h = rows // 2
    start = pl.multiple_of(core * h, PACKED_ROWS)
    return ref.at[pl.ds(start, h), :] if len(ref.shape) == 2 else ref.at[:, pl.ds(start, h), :]


def _shard_half(ref, shard, core, rows):
    h = rows // 2
    return ref.at[shard, pl.ds(pl.multiple_of(core * h, PACKED_ROWS), h), :]


ANY = pl.BlockSpec(memory_space=pl.ANY)


def _first_hop_copies(srcs, lands):
    x, y, c = _position()
    chip = 2 * x + y
    targets = [(px, py, c) for px, py in _chip_peers(x, y)] + [(x, y, 1 - c)]
    rows = srcs[0].shape[0]
    out = []
    for i, (s, l) in enumerate(zip(srcs, lands)):
        for k, dev in enumerate(targets):
            if i == 0 and k < 3:
                out.append((_half_rows(s, c, rows), _shard_half(l, chip, c, rows), dev, k))
            else:
                out.append((s, l.at[chip], dev, len(targets) * i + k))
    return out


def _second_hop_copies(srcs, lands):
    x, y, c = _position()
    rows = lands[0].shape[1]
    out = []
    for k, (px, py) in enumerate(_chip_peers(x, y)):
        half = _shard_half(lands[0], 2 * px + py, c, rows)
        out.append((half, half, (x, y, 1 - c), k))
    return out


HBM_SPEC = pl.BlockSpec(memory_space=pltpu.HBM)
SEM_SPEC = pl.BlockSpec(memory_space=pltpu.SEMAPHORE)
DATAFLOW = pltpu.SideEffectType.DATAFLOW_SIDE_EFFECTING


def _in_hbm(a):
    return pltpu.with_memory_space_constraint(a, pltpu.HBM)


SIBLING_HANDSHAKE_IDS = dict(grad_exchange_start_w_up=1, grad_assemble_start_others=2, grad_assemble_start_w_in=3)


def _split_start(name, groups, after, carry=None, sibling_only=False):
    spans, arrays = [], []
    for srcs, lands, _, _ in groups:
        spans.append((len(arrays), len(srcs), len(lands)))
        arrays += list(srcs) + list(lands)
    if carry is not None:
        arrays.append(carry)
    na, ng = len(arrays), len(groups)

    def body(*refs):
        sems, token = refs[na + 1:na + 1 + 2 * ng], refs[-1]
        if sibling_only:
            x, y, c = _position()
            barrier = pltpu.get_barrier_semaphore()
            pl.semaphore_signal(barrier, inc=1, device_id=(x, y, 1 - c), device_id_type=MESH)
            pl.semaphore_wait(barrier, 1)
        for g, (_, _, _, copies) in enumerate(groups):
            off, ns, nl = spans[g]
            for src, dst, dev, idx in copies(refs[off:off + ns], refs[off + ns:off + ns + nl]):
                pltpu.make_async_remote_copy(src_ref=src, dst_ref=dst, send_sem=sems[2 * g].at[idx], recv_sem=sems[2 * g + 1].at[idx],
                                             device_id=dev, device_id_type=MESH).start()
        token[...] = jnp.zeros_like(token)

    outs = pl.pallas_call(
        body, name=name,
        in_specs=[HBM_SPEC] * na + [ANY],
        out_specs=[SEM_SPEC] * (2 * ng) + [HBM_SPEC] * na + [pl.BlockSpec(memory_space=pltpu.VMEM)],
        out_shape=[pltpu.SemaphoreType.DMA((n_sems,)) for _, _, n_sems, _ in groups for _ in range(2)]
        + [pltpu.HBM(a.shape, a.dtype) for a in arrays] + [jax.ShapeDtypeStruct((SUBLANES, LANES), F32)],
        input_output_aliases={i: 2 * ng + i for i in range(na)},
        compiler_params=pltpu.CompilerParams(has_side_effects=DATAFLOW,
                                             collective_id=SIBLING_HANDSHAKE_IDS[name] if sibling_only else None),
    )(*[_in_hbm(a) for a in arrays], after)
    started = []
    for g, (off, ns, nl) in enumerate(spans):
        thru = outs[2 * ng + off:2 * ng + off + ns + nl]
        started.append(dict(send=outs[2 * g], recv=outs[2 * g + 1], srcs=list(thru[:ns]), lands=list(thru[ns:]),
                            tile=outs[-1], token=outs[-1][0, 0], carry=None if carry is None else outs[2 * ng + na - 1]))
    return started


def _split_wait(name, started, copies, after):
    n, m = len(started["srcs"]), len(started["lands"])
    after = list(after) if isinstance(after, (list, tuple)) else [after]

    def body(*refs):
        src_refs, land_refs = refs[:n], refs[n:n + m]
        send_sem, recv_sem = refs[n + m], refs[n + m + 1]
        for src, dst, dev, idx in copies(src_refs, land_refs):
            cp = pltpu.make_async_remote_copy(src_ref=src, dst_ref=dst, send_sem=send_sem.at[idx], recv_sem=recv_sem.at[idx],
                                              device_id=dev, device_id_type=MESH)
            cp.wait_send()
            cp.wait_recv()

    arrays = started["srcs"] + started["lands"]
    outs = pl.pallas_call(
        body, name=name,
        in_specs=[HBM_SPEC] * (n + m) + [SEM_SPEC, SEM_SPEC] + [ANY] * len(after),
        out_specs=[HBM_SPEC] * (n + m),
        out_shape=[pltpu.HBM(a.shape, a.dtype) for a in arrays],
        input_output_aliases={i: i for i in range(n + m)},
        compiler_params=pltpu.CompilerParams(has_side_effects=DATAFLOW),
    )(*arrays, started["send"], started["recv"], *after)
    return list(outs)


def _gather_copies(srcs, lands):
    x, y, c = _position()
    chip = 2 * x + y
    targets = [(px, py, c) for px, py in _chip_peers(x, y)] + [(x, y, 1 - c)]
    return [(s, l.at[chip], dev, len(targets) * i + k) for i, (s, l) in enumerate(zip(srcs, lands)) for k, dev in enumerate(targets)]


def _sibling_copies(srcs, lands):
    x, y, c = _position()
    return [(_half_rows(srcs[0], 1 - c, srcs[0].shape[1]), lands[0], (x, y, 1 - c), 0)]


def _sibling_whole_copies(srcs, lands):
    x, y, c = _position()
    return [(srcs[0], lands[0], (x, y, 1 - c), 0)]


def _exchange_copies(srcs, lands):
    x, y, c = _position()
    return [(srcs[0].at[2 * px + py], lands[0].at[k], (px, py, c), k) for k, (px, py) in enumerate(_chip_peers(x, y))]


def _pair_sum_call(grad, recv, chip_core, name):
    _, h, B = recv.shape
    tr = _row_tile(h, B)

    def body(cc_ref, g_ref, r_ref, o_ref, ob_ref):
        s = g_ref[...] + r_ref[...]
        ob_ref[...] = s.astype(BF16)

        @pl.when(pl.program_id(1) == cc_ref[0])
        def _():
            o_ref[...] = s

    g_spec = pl.BlockSpec((None, tr, B), lambda i, q, cc_ref: (q, cc_ref[1] * (h // tr) + i, 0))
    spec = pl.BlockSpec((None, tr, B), lambda i, q, cc_ref: (q, i, 0))
    own_spec = pl.BlockSpec((tr, B), lambda i, q, cc_ref: (i, 0))
    return pl.pallas_call(
        body, name=name,
        grid_spec=pltpu.PrefetchScalarGridSpec(num_scalar_prefetch=1, grid=(h // tr, N_CHIPS), in_specs=[g_spec, spec],
                                               out_specs=[own_spec, spec]),
        out_shape=[jax.ShapeDtypeStruct((h, B), F32), jax.ShapeDtypeStruct(recv.shape, BF16)],
        compiler_params=_params(("parallel", "arbitrary")),
    )(chip_core, grad, recv)


def _chip_sum_call(partial, recv, chip_core, name):
    _, h, B = recv.shape
    tr = _row_tile(h, B)

    def body(cc_ref, p_ref, r_ref, o_ref):
        o_ref[...] = ((p_ref[...] + r_ref[0].astype(F32)) + r_ref[1].astype(F32)) + r_ref[2].astype(F32)

    return pl.pallas_call(
        body, name=name,
        grid_spec=pltpu.PrefetchScalarGridSpec(
            num_scalar_prefetch=1, grid=(h // tr,),
            in_specs=[pl.BlockSpec((tr, B), lambda i, cc_ref: (i, 0)),
                      pl.BlockSpec((3, tr, B), lambda i, cc_ref: (0, i, 0))],
            out_specs=pl.BlockSpec((tr, B), lambda i, cc_ref: (cc_ref[1] * (h // tr) + i, 0))),
        out_shape=jax.ShapeDtypeStruct((2 * h, B), F32),
        compiler_params=_params(("parallel",)),
    )(chip_core, partial, recv)


def _assemble_copies(srcs, lands):
    x, y, c = _position()
    out = []
    for i, land in enumerate(lands):
        half = _half_rows(land, c, land.shape[0])
        out.append((half, half, (x, y, 1 - c), i))
    return out


N_DEVICES = 8


def _allsum_copies(srcs, lands):
    x, y, c = _position()
    me = 4 * x + 2 * y + c
    out = []
    for k in range(1, N_DEVICES):
        peer = (1 - x if k & 4 else x, 1 - y if k & 2 else y, 1 - c if k & 1 else c)
        out.append((srcs[0], lands[0].at[me], peer, k - 1))
    return out


def _ordered_sum_call(mine, landed, me_chip, shapes, sharded_cols):
    rows = mine.shape[0]
    outs = [(s[0], n) if n else s for s, n in zip(shapes, sharded_cols)]

    def body(mc_ref, x_ref, l_ref, *refs):
        acc_ref = refs[-1]
        acc = jnp.where(mc_ref[0] == 0, x_ref[...], l_ref[0])
        for d in range(1, N_DEVICES):
            acc = acc + jnp.where(mc_ref[0] == d, x_ref[...], l_ref[d])
        acc_ref[...] = acc
        first = 0
        for o_ref, (r, c), n in zip(refs[:-1], shapes, sharded_cols):
            per_row = c // LANES

            def unpack(chip, o_ref=o_ref, r=r, n=n, per_row=per_row, first=first):
                for i in range(r):
                    for j in range((n or per_row * LANES) // LANES):
                        src = first + i * per_row + chip * ((n or 0) // LANES) + j
                        o_ref[i:i + 1, j * LANES:(j + 1) * LANES] = acc_ref[src:src + 1, :]

            if n:
                for q in range(N_CHIPS):
                    pl.when(mc_ref[1] == q)(functools.partial(unpack, q))
            else:
                unpack(0)
            first += r * per_row

    results = pl.pallas_call(
        body, name="small_grad_sum",
        in_specs=[pl.BlockSpec(memory_space=pltpu.SMEM), pl.BlockSpec(memory_space=pltpu.VMEM), pl.BlockSpec(memory_space=pltpu.VMEM)],
        out_specs=[pl.BlockSpec(memory_space=pltpu.VMEM)] * len(outs),
        out_shape=[jax.ShapeDtypeStruct(s, F32) for s in outs],
        scratch_shapes=[pltpu.VMEM((rows, LANES), F32)],
    )(me_chip, mine, landed)
    return results


def _pack(arrays):
    flat = jnp.concatenate([a.reshape(-1).astype(F32) for a in arrays])
    rows = -(-flat.shape[0] // LANES)
    rows = -(-rows // SUBLANES) * SUBLANES
    flat = jnp.pad(flat, (0, rows * LANES - flat.shape[0]))
    return flat.reshape(rows, LANES)


def _local_step(xs, target, P, late_weights, on_grad):
    S, D = xs.shape
    qkv_width = 3 * N_HEADS * HEAD_DIM
    glu_col0, gate_col0 = qkv_width, qkv_width + 2 * D
    shard_major = lambda g: g.reshape(N_CHIPS, g.shape[0] // N_CHIPS, g.shape[1])

    h1 = _rms_fwd_call(xs, P["norm_mix_pre"])
    buckets = _bucket_tables()
    bias = _bias_table_call(P["rel_bias"] + 0.0 * h1[0, 0].astype(F32), buckets)
    P = dict(P, **late_weights("in", bias))
    proj = _matmul(h1, P["w_in"], "nn", "proj_in")
    parts = []
    for g in range(N_GROUPS):
        parts += _attn_fwd_call(proj, bias, g)
    a, a_bf, lse = _attn_merge_call(parts)
    P = dict(P, **late_weights("mix", a_bf))
    y_a = _matmul(a_bf, P["w_attn_out"], "nn", "attn_out")
    c1 = _conv_fwd_call(proj, glu_col0, P["conv_dw_w"], P["conv_dw_b"])
    cact = _ln_silu_call(c1, P["conv_ln_g"], P["conv_ln_b"])
    y_c = _matmul(cact, P["conv_pw_w"], "nn", "conv_pw")
    mixed = _mix_call(proj, gate_col0, P["b_gate"], y_a, y_c)
    out = _matmul(mixed, P["w_out"], "nn", "mix_out")
    x1, h2 = _res1_call(xs, out, P["norm_mix_post"], P["norm_ffn_pre"])
    P = dict(P, **late_weights("up", h2))
    u = _matmul(h2, P["w_up"], "nn", "ffn_up")
    f = _ffn_fwd_call(u, P["ffn_conv_w"], P["ffn_conv_b"])
    P = dict(P, **late_weights("down", f))
    yff = _matmul(f, P["w_down"], "nn", "ffn_down")
    loss_tile, dx2, dyff, dg_ffn_post = _loss_call(yff, x1, P["norm_ffn_post"], target)

    G = {}
    G["norm_ffn_post"] = dg_ffn_post
    on_grad("w_down", shard_major(_matmul(f, dyff, "tn", "ffn_down_dw")))
    df = _matmul(dyff, P["w_down"], "nt", "ffn_down_dx")
    du, dwg, dwv, dbg, dbv = _ffn_bwd_call(u, P["ffn_conv_w"], P["ffn_conv_b"], df)
    G["ffn_conv_w"] = jnp.concatenate([dwg[:FFN_CONV_WIDTH], dwv[:FFN_CONV_WIDTH]], axis=1)
    G["ffn_conv_b"] = jnp.concatenate([dbg, dbv], axis=1)
    du = on_grad("w_up", functools.partial(_grad_half_matmul, h2, "ffn_up_dw"), carry=du)
    dh2 = _matmul(du, P["w_up"], "nt", "ffn_up_dx")
    dx1, dout, G["norm_ffn_pre"], G["norm_mix_post"] = _mid_bwd_call(x1, P["norm_ffn_pre"], dh2, dx2, out, P["norm_mix_post"])
    on_grad("w_out", shard_major(_matmul(mixed, dout, "tn", "mix_out_dw")))
    dmixed = _matmul(dout, P["w_out"], "nt", "mix_out_dx")
    dya, dyc, dproj, dba, dbc = _mix_bwd_call(dmixed, proj, gate_col0, P["b_gate"], y_a, y_c)
    G["b_gate"] = jnp.concatenate([dba, dbc], axis=1)
    on_grad("w_attn_out", _matmul(a_bf, dya, "tn", "attn_out_dw", out_shards=True))
    dyc = on_grad("conv_pw_w", shard_major(_matmul(cact, dyc, "tn", "conv_pw_dw")), carry=dyc)
    da = _matmul(dya, P["w_attn_out"], "nt", "attn_out_dx")
    dcact = _matmul(dyc, P["conv_pw_w"], "nt", "conv_pw_dx")
    dc1, G["conv_ln_g"], G["conv_ln_b"] = _ln_silu_bwd_call(c1, P["conv_ln_g"], P["conv_ln_b"], dcact)
    dproj, dw_dw, G["conv_dw_b"] = _conv_bwd_call(proj, glu_col0, P["conv_dw_w"], dc1, dproj)
    G["conv_dw_w"] = dw_dw[:CONV_WIDTH]
    dbs = []
    for g in range(N_GROUPS):
        dproj, db = _attn_bwd_call(proj, bias, a, da, lse, g, dproj)
        dbs.append(db)
    G["rel_bias"] = _bias_grad_call(jnp.concatenate(dbs, axis=0), buckets)
    dproj = on_grad("w_in", functools.partial(_grad_half_matmul, h1, "proj_in_dw"), carry=dproj)
    dh1 = _matmul(dproj, P["w_in"], "nt", "proj_in_dx")
    dh1 = on_grad(None, None, carry=dh1)
    grad_x, G["norm_mix_pre"] = _in_bwd_call(xs, P["norm_mix_pre"], dh1, dx1)
    return loss_tile, grad_x, G


def kernel(x, w_in, b_gate, rel_bias, w_attn_out, conv_dw_w, conv_dw_b, conv_ln_g, conv_ln_b, conv_pw_w, w_out, norm_mix_pre, norm_mix_post, norm_ffn_pre, norm_ffn_post, w_up, ffn_conv_w, ffn_conv_b, w_down, loss_target, m_w_in, m_b_gate, m_rel_bias, m_w_attn_out, m_conv_dw_w, m_conv_dw_b, m_conv_ln_g, m_conv_ln_b, m_conv_pw_w, m_w_out, m_norm_mix_pre, m_norm_mix_post, m_norm_ffn_pre, m_norm_ffn_post, m_w_up, m_ffn_conv_w, m_ffn_conv_b, m_w_down, v_w_in, v_b_gate, v_rel_bias, v_w_attn_out, v_conv_dw_w, v_conv_dw_b, v_conv_ln_g, v_conv_ln_b, v_conv_pw_w, v_w_out, v_norm_mix_pre, v_norm_mix_post, v_norm_ffn_pre, v_norm_ffn_post, v_w_up, v_ffn_conv_w, v_ffn_conv_b, v_w_down):
    weights = dict(w_in=w_in, b_gate=b_gate, rel_bias=rel_bias, w_attn_out=w_attn_out, conv_dw_w=conv_dw_w, conv_dw_b=conv_dw_b,
                   conv_ln_g=conv_ln_g, conv_ln_b=conv_ln_b, conv_pw_w=conv_pw_w, w_out=w_out, norm_mix_pre=norm_mix_pre,
                   norm_mix_post=norm_mix_post, norm_ffn_pre=norm_ffn_pre, norm_ffn_post=norm_ffn_post, w_up=w_up,
                   ffn_conv_w=ffn_conv_w, ffn_conv_b=ffn_conv_b, w_down=w_down)
    m_in = dict(w_in=m_w_in, b_gate=m_b_gate, rel_bias=m_rel_bias, w_attn_out=m_w_attn_out, conv_dw_w=m_conv_dw_w,
                conv_dw_b=m_conv_dw_b, conv_ln_g=m_conv_ln_g, conv_ln_b=m_conv_ln_b, conv_pw_w=m_conv_pw_w, w_out=m_w_out,
                norm_mix_pre=m_norm_mix_pre, norm_mix_post=m_norm_mix_post, norm_ffn_pre=m_norm_ffn_pre,
                norm_ffn_post=m_norm_ffn_post, w_up=m_w_up, ffn_conv_w=m_ffn_conv_w, ffn_conv_b=m_ffn_conv_b, w_down=m_w_down)
    v_in = dict(w_in=v_w_in, b_gate=v_b_gate, rel_bias=v_rel_bias, w_attn_out=v_w_attn_out, conv_dw_w=v_conv_dw_w,
                conv_dw_b=v_conv_dw_b, conv_ln_g=v_conv_ln_g, conv_ln_b=v_conv_ln_b, conv_pw_w=v_conv_pw_w, w_out=v_w_out,
                norm_mix_pre=v_norm_mix_pre, norm_mix_post=v_norm_mix_post, norm_ffn_pre=v_norm_ffn_pre,
                norm_ffn_post=v_norm_ffn_post, w_up=v_w_up, ffn_conv_w=v_ffn_conv_w, ffn_conv_b=v_ffn_conv_b, w_down=v_w_down)
    names = list(weights)
    xi, yi, ci = _position()
    chip = 2 * xi + yi
    core_arr = jnp.reshape(ci, (1,)).astype(jnp.int32)

    xs = x[0]
    target = loss_target[0]
    S, D = xs.shape

    big = ["w_in", "w_attn_out", "conv_pw_w", "w_out", "w_up", "w_down"]
    row_sharded = ("conv_pw_w", "w_out", "w_down")
    natural = lambda k, g: g.reshape(-1, g.shape[2]) if k in row_sharded else g
    first_srcs = [w_in[0].astype(BF16), conv_dw_w[0], ffn_conv_w[0]]
    first_lands = [lax.empty((N_CHIPS,) + s.shape, s.dtype) for s in first_srcs]
    (first_hop,) = _split_start("gather_in_start", [(first_srcs, first_lands, 4 * len(first_srcs), _first_hop_copies)], core_arr)
    launched = first_hop["token"]
    late_sets = dict(mix=["w_attn_out", "conv_pw_w", "w_out"], up=["w_up"], down=["w_down"])
    late_groups = []
    for keys in late_sets.values():
        srcs = [(weights[k][0] + launched).astype(BF16) for k in keys]
        late_groups.append((srcs, [lax.empty((N_CHIPS,) + s.shape, BF16) for s in srcs], 4 * len(keys), _gather_copies))
    started = {}

    def late_weights(tag, after):
        if tag == "in":
            casts = [s for srcs, _, _, _ in late_groups for s in srcs]
            w_in_halves, dw4, fc4 = _split_wait("gather_in_wait", first_hop, _first_hop_copies, [after] + casts)[len(first_srcs):]
            second_hop, *late = _split_start("gather_in_pass_start", [([], [w_in_halves], 3, _second_hop_copies)] + late_groups, dw4)
            started.update(zip(late_sets, late))
            (w_in_full,) = _split_wait("gather_in_pass_wait", second_hop, _second_hop_copies, second_hop["tile"])
            return dict(w_in=w_in_full, conv_dw_w=jnp.concatenate(list(dw4), axis=1), ffn_conv_w=jnp.concatenate(list(fc4), axis=1))
        landed = _split_wait(f"gather_{tag}_wait", started[tag], _gather_copies, after)[len(late_sets[tag]):]
        return {k: natural(k, g) for k, g in zip(late_sets[tag], landed)}

    chip_core = jnp.stack([chip, ci]).astype(jnp.int32)
    exchanging, pending, second_half = {}, {}, {}

    held = []

    def launch(tag, after, carry=None):
        keys, groups, partial = [], [], {}
        for k in list(exchanging):
            st, copies = exchanging.pop(k)
            gk, r1 = _split_wait(f"sibling_exchange_wait_{k}", st, copies, after)
            if k in second_half:
                partial[k], s16 = second_half.pop(k)(init=r1)
            else:
                partial[k], s16 = _pair_sum_call(gk, r1, chip_core, f"pair_sum_{k}")
            keys.append(k)
            groups.append(([s16], [lax.empty((3,) + s16.shape[1:], BF16)], 3, _exchange_copies))
        fresh = [(k, copies) for k, _, copies in held]
        for _, g3, copies in held:
            rows = g3.shape[1] // 2 if copies is _sibling_copies else g3.shape[1]
            groups.append(([g3], [lax.empty((N_CHIPS, rows, g3.shape[2]), F32)], 1, copies))
        held.clear()
        begun = _split_start(f"grad_exchange_start_{tag}", groups, core_arr, carry, sibling_only=not keys)
        for k, st in zip(keys, begun):
            pending[k] = (partial[k], st)
        for (k, copies), st in zip(fresh, begun[len(keys):]):
            exchanging[k] = (st, copies)
        return begun[0]["carry"]

    others = [k for k in big if k != "w_in"]
    assembling = {}

    def on_grad(k, g, carry=None):
        if k is None:
            carried = launch("last", carry[:SUBLANES, :LANES], carry)
            assembling["others"] = assemble_start(others, carried, "others", carried)
            return assembling["others"]["carry"]
        if callable(g):
            theirs = g(jnp.stack([chip, 1 - ci]).astype(jnp.int32), carry)
            held.append((k, theirs, _sibling_whole_copies))
            carried = launch(k, theirs[0, :SUBLANES, :LANES], carry)
            second_half[k] = functools.partial(g, chip_core, carried)
            return carried
        held.append((k, g, _sibling_copies))
        if k in ("w_down", "w_out", "w_attn_out"):
            return carry
        return launch(k, g[0, :SUBLANES, :LANES], carry)

    def assemble_start(keys, after, tag, carry=None):
        halves = []
        for k in keys:
            s32, st = pending[k]
            recv2 = _split_wait(f"chip_exchange_wait_{k}", st, _exchange_copies, after)[1]
            halves.append(_chip_sum_call(s32, recv2, chip_core, f"chip_sum_{k}"))
        (st,) = _split_start(f"grad_assemble_start_{tag}", [([], halves, len(halves), _assemble_copies)], core_arr, carry,
                             sibling_only=True)
        return st

    def assemble_wait(keys, st, after, tag):
        return dict(zip(keys, _split_wait(f"grad_assemble_wait_{tag}", st, _assemble_copies, after)))

    P = dict(b_gate=b_gate, rel_bias=rel_bias, conv_dw_b=conv_dw_b, conv_ln_g=conv_ln_g, conv_ln_b=conv_ln_b,
             norm_mix_pre=norm_mix_pre + launched, norm_mix_post=norm_mix_post, norm_ffn_pre=norm_ffn_pre,
             norm_ffn_post=norm_ffn_post, ffn_conv_b=ffn_conv_b)
    loss_tile, grad_x, G = _local_step(xs, target, P, late_weights, on_grad)

    small = [k for k in names if k not in big]
    packed = _pack([loss_tile[:1]] + [G[k] for k in small])
    (allsum,) = _split_start("small_grad_allsum_start",
                             [([packed], [jnp.zeros((N_DEVICES,) + packed.shape, F32)], N_DEVICES - 1, _allsum_copies)], core_arr)

    reduced, grads, deltas, new_m, new_v = {}, {}, {}, {}, {}

    def update(keys):
        for k in keys:
            gk, d, mn, vn = _adamw_call(weights[k][0], reduced[k], m_in[k][0], v_in[k][0], f"adamw_{k}")
            grads[k], deltas[k], new_m[k], new_v[k] = gk[None], d[None], mn[None], vn[None]

    reduced.update(assemble_wait(others, assembling["others"], [allsum["tile"], grad_x], "others"))
    update(others)
    assembling["w_in"] = assemble_start(["w_in"], [deltas[k] for k in others], "w_in")

    me_chip = jnp.stack([4 * xi + 2 * yi + ci, chip]).astype(jnp.int32)
    mine, landed = _split_wait("small_grad_allsum_wait", allsum, _allsum_copies, assembling["w_in"]["tile"])
    piece_shapes = [(1, LANES)] + [(G[k].size // LANES, LANES) if k == "rel_bias" else G[k].shape for k in small]
    piece_cols = [0] + [weights[k].shape[2] if k in ("conv_dw_w", "ffn_conv_w") else 0 for k in small]
    loss_row, *summed = _ordered_sum_call(mine, landed, me_chip, piece_shapes, piece_cols)
    loss = loss_row[0, 0]
    for k, gsum in zip(small, summed):
        grads[k] = gsum.reshape(weights[k].shape)
    ds, mns, vns = _adamw_small_call([weights[k] for k in small], [grads[k] for k in small],
                                     [m_in[k] for k in small], [v_in[k] for k in small])
    deltas.update(zip(small, ds))
    new_m.update(zip(small, mns))
    new_v.update(zip(small, vns))
    reduced.update(assemble_wait(["w_in"], assembling["w_in"], list(ds), "w_in"))
    update(["w_in"])

    return (loss, grad_x[None], *[grads[k] for k in names], *[deltas[k] for k in names],
            *[new_m[k] for k in names], *[new_v[k] for k in names])
```

```python
import functools
import math

import jax
import jax.numpy as jnp
import numpy as np
from jax import lax
from jax.experimental import pallas as pl
from jax.experimental.pallas import tpu as pltpu

F32 = jnp.float32
BF16 = jnp.bfloat16
MESH = pl.DeviceIdType.MESH

HEAD_DIM = 128
HEADS_PER_GROUP = 4
DILATED_PATTERNS = ((128, 1), (512, 4), (2048, 16))
N_GROUPS = 3
N_HEADS = N_GROUPS * HEADS_PER_GROUP
SPAN = 128
GROUP_WIDTH = HEADS_PER_GROUP * HEAD_DIM
CONV_WIDTH = 31
FFN_CONV_WIDTH = 3
N_BUCKETS = 32
MAX_DISTANCE = 2048
RMS_EPS = 1e-6
LN_EPS = 1e-5
NEG_INF = -1e30
ADAM_LR = 0.001
ADAM_B1 = 0.9
ADAM_B2 = 0.999
ADAM_EPS = 1e-08
ADAM_WD = 0.01
ADAM_STEP = 10

LANES = 128
SUBLANES = 8
PACKED_ROWS = 16
ROW_TILE = 512
GATE_ROWS, GATE_COLS = 512, 512
TIME_BLOCK = 128
CONV_PAD = 32
FFN_PAD = 8
VMEM_LIMIT = 56 << 20


def _params(sem=None, vmem=None):
    kw = {}
    if sem is not None:
        kw["dimension_semantics"] = sem
    if vmem is not None:
        kw["vmem_limit_bytes"] = vmem
    return pltpu.CompilerParams(**kw)


def _pick(n, cands):
    for c in cands:
        if n % c == 0:
            return c
    return n


ELEMENTWISE_TILE_BYTES = 3 << 19


def _row_tile(rows, cols):
    for align in (16, SUBLANES):
        fits = [t for t in range(align, rows + 1, align) if rows % t == 0 and t * cols * 4 <= ELEMENTWISE_TILE_BYTES]
        if fits:
            return max(fits)
    return SUBLANES


N_CHIPS = 4
M_TILES = (1024, 1408, 512, 256, 128)
N_TILES = (1024, 512, 1408, 256, 128)
K_TILES = (2176, 2048, 1408, 1024, 512, 256, 128)


def _matmul(a, b, mode, name, out_shards=False, tm=None):
    assert a.dtype == BF16 and b.dtype == BF16, (name, a.dtype, b.dtype)
    b3 = b.ndim == 3
    tn = tk = None
    halves = None
    if mode == "nn":
        M, K = a.shape
        N = b.shape[-1] * (N_CHIPS if b3 else 1)
        tn = b.shape[-1] if b3 else None
    elif mode == "nt":
        if a.ndim == 3:
            halves = a.shape[2]
        M, K = a.shape[-2], a.shape[-1] * (a.shape[0] if a.ndim == 3 else 1)
        N = b.shape[-2]
        tk = b.shape[-1] if b3 else None
    else:
        if b3:
            halves = b.shape[2]
        K, M = a.shape
        N = b.shape[-1] * (b.shape[0] if b3 else 1)
        tn = N // N_CHIPS if out_shards else None
    tm = tm or _pick(M, M_TILES)
    tn = tn or _pick(N, N_TILES)
    tk = tk or _pick(K, K_TILES)
    nk = K // tk
    dn = {"nn": (((1,), (0,)), ((), ())), "nt": (((1,), (1,)), ((), ())), "tn": (((0,), (0,)), ((), ()))}[mode]

    def body(a_ref, b_ref, o_ref):
        if nk == 1:
            o_ref[...] = lax.dot_general(a_ref[...], b_ref[...], dn, preferred_element_type=F32)
        else:
            @pl.when(pl.program_id(2) == 0)
            def _():
                o_ref[...] = jnp.zeros_like(o_ref)

            o_ref[...] += lax.dot_general(a_ref[...], b_ref[...], dn, preferred_element_type=F32)

    if mode == "tn":
        a_spec = pl.BlockSpec((tk, tm), lambda i, j, k: (k, i))
    elif halves:
        per = halves // tk
        a_spec = pl.BlockSpec((None, tm, tk), lambda i, j, k: (k // per, i, k % per))
    else:
        a_spec = pl.BlockSpec((tm, tk), lambda i, j, k: (i, k))
    if mode == "nn":
        b_spec = pl.BlockSpec((None, tk, tn), lambda i, j, k: (j, k, 0)) if b3 else pl.BlockSpec((tk, tn), lambda i, j, k: (k, j))
    elif mode == "nt":
        b_spec = pl.BlockSpec((None, tn, tk), lambda i, j, k: (k, j, 0)) if b3 else pl.BlockSpec((tn, tk), lambda i, j, k: (j, k))
    elif halves:
        per = halves // tn
        b_spec = pl.BlockSpec((None, tk, tn), lambda i, j, k: (j // per, k, j % per))
    else:
        b_spec = pl.BlockSpec((tk, tn), lambda i, j, k: (k, j))
    if out_shards:
        out_spec = pl.BlockSpec((None, tm, tn), lambda i, j, k: (j, i, 0))
        out_shape = jax.ShapeDtypeStruct((N_CHIPS, M, tn), F32)
    else:
        out_spec = pl.BlockSpec((tm, tn), lambda i, j, k: (i, j))
        out_shape = jax.ShapeDtypeStruct((M, N), F32)
    return pl.pallas_call(
        body, name=name, grid=(M // tm, N // tn, nk),
        in_specs=[a_spec, b_spec], out_specs=out_spec, out_shape=out_shape,
        compiler_params=_params(("parallel", "parallel", "arbitrary"), VMEM_LIMIT),
    )(a, b)


def _grad_half_matmul(a, name, chip_half, b, init=None):
    K, M = a.shape
    parts = b.ndim == 3
    N = b.shape[-1] * (b.shape[0] if parts else 1)
    h, tn = M // 2, N // N_CHIPS
    summed = init is not None
    dn = (((0,), (0,)), ((), ()))

    def body(ch_ref, a_ref, b_ref, *rest):
        product = lax.dot_general(a_ref[...], b_ref[...], dn, preferred_element_type=F32)
        if not summed:
            rest[0][...] = product
            return
        init_ref, own_ref, sum16_ref = rest
        total = product + init_ref[...]
        sum16_ref[...] = total.astype(BF16)

        @pl.when(pl.program_id(0) == ch_ref[0])
        def _():
            own_ref[...] = total

    if parts:
        per = b.shape[2] // tn
        b_spec = pl.BlockSpec((None, K, tn), lambda j, ch_ref: (j // per, 0, j % per))
    else:
        b_spec = pl.BlockSpec((K, tn), lambda j, ch_ref: (0, j))
    shard_spec = pl.BlockSpec((None, h, tn), lambda j, ch_ref: (j, 0, 0))
    own_spec = pl.BlockSpec((h, tn), lambda j, ch_ref: (0, 0))
    shape = (N_CHIPS, h, tn)
    return pl.pallas_call(
        body, name=name + ("_mine" if summed else "_theirs"),
        grid_spec=pltpu.PrefetchScalarGridSpec(
            num_scalar_prefetch=1, grid=(N_CHIPS,),
            in_specs=[pl.BlockSpec((K, h), lambda j, ch_ref: (0, ch_ref[1])), b_spec] + [shard_spec] * summed,
            out_specs=[own_spec, shard_spec] if summed else shard_spec),
        out_shape=[jax.ShapeDtypeStruct((h, tn), F32), jax.ShapeDtypeStruct(shape, BF16)] if summed
        else jax.ShapeDtypeStruct(shape, F32),
        compiler_params=_params(("arbitrary",), VMEM_LIMIT),
    )(chip_half, a, b, *([init] if summed else []))


def _rms(x, g):
    r = lax.rsqrt(jnp.mean(x * x, axis=-1, keepdims=True) + RMS_EPS)
    return x * r * g


def _rms_bwd(x, g, dy):
    r = lax.rsqrt(jnp.mean(x * x, axis=-1, keepdims=True) + RMS_EPS)
    n = x * r
    dn = dy * g
    dx = r * (dn - n * jnp.mean(dn * n, axis=-1, keepdims=True))
    return dx, jnp.sum(dy * n, axis=0, keepdims=True)


def _sigmoid(x):
    return 1.0 / (1.0 + jnp.exp(-x))


_GELU_C = math.sqrt(2.0 / math.pi)


def _gelu(x):
    return 0.5 * x * (1.0 + jnp.tanh(_GELU_C * (x + 0.044715 * x * x * x)))


def _gelu_and_grad(x):
    x2 = x * x
    t = jnp.tanh(_GELU_C * x * (1.0 + 0.044715 * x2))
    half = 0.5 * (1.0 + t)
    return x * half, half + (0.5 * _GELU_C) * x * (1.0 - t * t) * (1.0 + (3.0 * 0.044715) * x2)


def _row_spec(width, col_block=0):
    return pl.BlockSpec((ROW_TILE, width), lambda i: (i, col_block))


def _vec_spec(width, col_block=0):
    return pl.BlockSpec((1, width), lambda i: (0, col_block))


def _accumulate(ref, part):
    @pl.when(pl.program_id(0) == 0)
    def _():
        ref[...] = part

    @pl.when(pl.program_id(0) > 0)
    def _():
        ref[...] += part


def _rms_fwd_call(x, g):
    S, D = x.shape

    def body(x_ref, g_ref, h_ref):
        h_ref[...] = _rms(x_ref[...], g_ref[...]).astype(BF16)

    return pl.pallas_call(
        body, name="rms_mix_pre", grid=(S // ROW_TILE,),
        in_specs=[_row_spec(D), _vec_spec(D)], out_specs=_row_spec(D),
        out_shape=jax.ShapeDtypeStruct((S, D), BF16),
        compiler_params=_params(("parallel",)),
    )(x, g)


def _ln_silu_call(c1, g, b):
    S, C = c1.shape

    def body(c_ref, g_ref, b_ref, o_ref):
        xv = c_ref[...]
        mu = jnp.mean(xv, axis=-1, keepdims=True)
        xc = xv - mu
        var = jnp.mean(xc * xc, axis=-1, keepdims=True)
        z = xc * lax.rsqrt(var + LN_EPS) * g_ref[...] + b_ref[...]
        o_ref[...] = (z * _sigmoid(z)).astype(BF16)

    return pl.pallas_call(
        body, name="conv_ln_silu", grid=(S // ROW_TILE,),
        in_specs=[_row_spec(C), _vec_spec(C), _vec_spec(C)], out_specs=_row_spec(C),
        out_shape=jax.ShapeDtypeStruct((S, C), BF16),
        compiler_params=_params(("parallel",)),
    )(c1, g, b)


def _ln_silu_bwd_call(c1, g, b, dc):
    S, C = c1.shape

    def body(c_ref, g_ref, b_ref, dc_ref, dx_ref, dg_ref, db_ref):
        xv = c_ref[...]
        mu = jnp.mean(xv, axis=-1, keepdims=True)
        xc = xv - mu
        rs = lax.rsqrt(jnp.mean(xc * xc, axis=-1, keepdims=True) + LN_EPS)
        xh = xc * rs
        z = xh * g_ref[...] + b_ref[...]
        sg = _sigmoid(z)
        dz = dc_ref[...] * (sg * (1.0 + z * (1.0 - sg)))
        dxh = dz * g_ref[...]
        dx_ref[...] = rs * (dxh - jnp.mean(dxh, axis=-1, keepdims=True) - xh * jnp.mean(dxh * xh, axis=-1, keepdims=True))
        _accumulate(dg_ref, jnp.sum(dz * xh, axis=0, keepdims=True))
        _accumulate(db_ref, jnp.sum(dz, axis=0, keepdims=True))

    return pl.pallas_call(
        body, name="conv_ln_silu_bwd", grid=(S // ROW_TILE,),
        in_specs=[_row_spec(C), _vec_spec(C), _vec_spec(C), _row_spec(C)],
        out_specs=[_row_spec(C), _vec_spec(C), _vec_spec(C)],
        out_shape=[jax.ShapeDtypeStruct((S, C), F32), jax.ShapeDtypeStruct((1, C), F32), jax.ShapeDtypeStruct((1, C), F32)],
        compiler_params=_params(("arbitrary",)),
    )(c1, g, b, dc)


def _mix_call(proj, gate_col0, b_gate, y_a, y_c):
    S, D = y_a.shape
    w = GATE_COLS
    nc = D // w
    ga0, gc0 = gate_col0 // w, (gate_col0 + D) // w

    def body(ga_ref, gc_ref, ba_ref, bc_ref, ya_ref, yc_ref, o_ref):
        o_ref[...] = (_sigmoid(ga_ref[...] + ba_ref[...]) * ya_ref[...]
                      + _sigmoid(gc_ref[...] + bc_ref[...]) * yc_ref[...]).astype(BF16)

    tile = lambda off: pl.BlockSpec((GATE_ROWS, w), lambda i, j: (i, off + j))
    vec = lambda off: pl.BlockSpec((1, w), lambda i, j: (0, off + j))
    return pl.pallas_call(
        body, name="gate_mix", grid=(S // GATE_ROWS, nc),
        in_specs=[tile(ga0), tile(gc0), vec(0), vec(nc), tile(0), tile(0)],
        out_specs=tile(0), out_shape=jax.ShapeDtypeStruct((S, D), BF16),
        compiler_params=_params(("parallel", "parallel")),
    )(proj, proj, b_gate, b_gate, y_a, y_c)


def _window_stores(stage_ref, slot, dst_ref, rows, cols, sems):
    width = stage_ref.shape[-1]
    return [pltpu.make_async_copy(stage_ref.at[slot, p], dst_ref.at[rows, pl.ds(pl.multiple_of(c, LANES), width)], sems.at[slot, p])
            for p, c in enumerate(cols)]


def _staged_window_stores(stage_ref, dst_ref, sems, step, n_steps, rows, cols, fill):
    slot = step % 2
    copies = lambda s: _window_stores(stage_ref, s, dst_ref, rows, cols, sems)

    @pl.when(step >= 2)
    def _():
        for cp in copies(slot):
            cp.wait()

    fill(slot)
    for cp in copies(slot):
        cp.start()

    @pl.when(step == n_steps - 1)
    def _():
        for s in ([slot, 1 - slot] if n_steps > 1 else [slot]):
            for cp in copies(s):
                cp.wait()


def _mix_bwd_call(dmixed, proj, gate_col0, b_gate, y_a, y_c):
    S, D = y_a.shape
    w = GATE_COLS
    nc = D // w
    nr = S // GATE_ROWS
    ga0, gc0 = gate_col0 // w, (gate_col0 + D) // w

    def body(dm_ref, ga_ref, gc_ref, ba_ref, bc_ref, ya_ref, yc_ref, dya_ref, dyc_ref, dproj_ref, dba_ref, dbc_ref,
             stage_ref, sems):
        j, i = pl.program_id(0), pl.program_id(1)
        dm = dm_ref[...]
        sa = _sigmoid(ga_ref[...] + ba_ref[...])
        sc = _sigmoid(gc_ref[...] + bc_ref[...])
        dya_ref[...] = (dm * sa).astype(BF16)
        dyc_ref[...] = (dm * sc).astype(BF16)
        dga = dm * ya_ref[...] * sa * (1.0 - sa)
        dgc = dm * yc_ref[...] * sc * (1.0 - sc)

        def fill(slot):
            stage_ref[slot, 0] = dga.astype(BF16)
            stage_ref[slot, 1] = dgc.astype(BF16)

        rows = pl.ds(pl.multiple_of(i * GATE_ROWS, GATE_ROWS), GATE_ROWS)
        _staged_window_stores(stage_ref, dproj_ref, sems, j * nr + i, nc * nr, rows,
                              [gate_col0 + j * w, gate_col0 + D + j * w], fill)
        pa = jnp.sum(dga, axis=0, keepdims=True)
        pc = jnp.sum(dgc, axis=0, keepdims=True)

        @pl.when(i == 0)
        def _():
            dba_ref[...] = pa
            dbc_ref[...] = pc

        @pl.when(i > 0)
        def _():
            dba_ref[...] += pa
            dbc_ref[...] += pc

    tile = lambda off: pl.BlockSpec((GATE_ROWS, w), lambda j, i: (i, off + j))
    vec = lambda off: pl.BlockSpec((1, w), lambda j, i: (0, off + j))
    return pl.pallas_call(
        body, name="gate_mix_bwd", grid=(nc, nr),
        in_specs=[tile(0), tile(ga0), tile(gc0), vec(0), vec(nc), tile(0), tile(0)],
        out_specs=[tile(0), tile(0), ANY, vec(0), vec(0)],
        out_shape=[jax.ShapeDtypeStruct((S, D), BF16)] * 2 + [jax.ShapeDtypeStruct((S, proj.shape[1]), BF16)] + [
                   jax.ShapeDtypeStruct((1, D), F32), jax.ShapeDtypeStruct((1, D), F32)],
        scratch_shapes=[pltpu.VMEM((2, 2, GATE_ROWS, w), BF16), pltpu.SemaphoreType.DMA((2, 2))],
        compiler_params=_params(("arbitrary", "arbitrary")),
    )(dmixed, proj, proj, b_gate, b_gate, y_a, y_c)


def _res1_call(x, out, g_post, g_pre):
    S, D = x.shape

    def body(x_ref, o_ref, gp_ref, gq_ref, x1_ref, h2_ref):
        x1 = x_ref[...] + _rms(o_ref[...], gp_ref[...])
        x1_ref[...] = x1
        h2_ref[...] = _rms(x1, gq_ref[...]).astype(BF16)

    return pl.pallas_call(
        body, name="residual_mix", grid=(S // ROW_TILE,),
        in_specs=[_row_spec(D), _row_spec(D), _vec_spec(D), _vec_spec(D)],
        out_specs=[_row_spec(D), _row_spec(D)],
        out_shape=[jax.ShapeDtypeStruct((S, D), F32), jax.ShapeDtypeStruct((S, D), BF16)],
        compiler_params=_params(("parallel",)),
    )(x, out, g_post, g_pre)


def _loss_call(y, x1, g_post, target):
    S, D = y.shape

    def body(y_ref, x1_ref, g_ref, t_ref, loss_ref, dx_ref, dy_ref, dg_ref):
        yv, gv = y_ref[...], g_ref[...]
        err = x1_ref[...] + _rms(yv, gv) - t_ref[...]
        dx2 = err * (1.0 / D)
        dx_ref[...] = dx2
        dy, dg = _rms_bwd(yv, gv, dx2)
        dy_ref[...] = dy.astype(BF16)
        _accumulate(dg_ref, dg)
        part = 0.5 * jnp.sum(jnp.mean(err * err, axis=-1, keepdims=True), axis=0, keepdims=True)
        _accumulate(loss_ref, jnp.broadcast_to(part, (SUBLANES, LANES)))

    return pl.pallas_call(
        body, name="residual_ffn_loss", grid=(S // ROW_TILE,),
        in_specs=[_row_spec(D), _row_spec(D), _vec_spec(D), _row_spec(D)],
        out_specs=[pl.BlockSpec((SUBLANES, LANES), lambda i: (0, 0)), _row_spec(D), _row_spec(D), _vec_spec(D)],
        out_shape=[jax.ShapeDtypeStruct((SUBLANES, LANES), F32), jax.ShapeDtypeStruct((S, D), F32),
                   jax.ShapeDtypeStruct((S, D), BF16), jax.ShapeDtypeStruct((1, D), F32)],
        compiler_params=_params(("arbitrary",)),
    )(y, x1, g_post, target)


def _mid_bwd_call(x1, g_pre, dh2, dx2, out, g_post):
    S, D = x1.shape

    def body(x1_ref, gq_ref, dh_ref, dx2_ref, o_ref, gp_ref, dx1_ref, do_ref, dgq_ref, dgp_ref):
        d, dgq = _rms_bwd(x1_ref[...], gq_ref[...], dh_ref[...])
        dx1 = dx2_ref[...] + d
        dx1_ref[...] = dx1
        do, dgp = _rms_bwd(o_ref[...], gp_ref[...], dx1)
        do_ref[...] = do.astype(BF16)
        _accumulate(dgq_ref, dgq)
        _accumulate(dgp_ref, dgp)

    return pl.pallas_call(
        body, name="residual_mix_bwd", grid=(S // ROW_TILE,),
        in_specs=[_row_spec(D), _vec_spec(D), _row_spec(D), _row_spec(D), _row_spec(D), _vec_spec(D)],
        out_specs=[_row_spec(D), _row_spec(D), _vec_spec(D), _vec_spec(D)],
        out_shape=[jax.ShapeDtypeStruct((S, D), F32), jax.ShapeDtypeStruct((S, D), BF16)] + [jax.ShapeDtypeStruct((1, D), F32)] * 2,
        compiler_params=_params(("arbitrary",)),
    )(x1, g_pre, dh2, dx2, out, g_post)


def _in_bwd_call(x, g, dh1, dx1):
    S, D = x.shape

    def body(x_ref, g_ref, dh_ref, dx1_ref, gx_ref, dg_ref):
        d, dg = _rms_bwd(x_ref[...], g_ref[...], dh_ref[...])
        gx_ref[...] = dx1_ref[...] + d
        _accumulate(dg_ref, dg)

    return pl.pallas_call(
        body, name="rms_mix_pre_bwd", grid=(S // ROW_TILE,),
        in_specs=[_row_spec(D), _vec_spec(D), _row_spec(D), _row_spec(D)],
        out_specs=[_row_spec(D), _vec_spec(D)],
        out_shape=[jax.ShapeDtypeStruct((S, D), F32), jax.ShapeDtypeStruct((1, D), F32)],
        compiler_params=_params(("arbitrary",)),
    )(x, g, dh1, dx1)


def _bucket_table(dilation):
    qi = np.arange(SPAN)[:, None]
    ki = np.arange(2 * SPAN)[None, :]
    dist = np.maximum(qi + SPAN - ki, 0) * dilation
    max_exact = N_BUCKETS // 2
    d = np.maximum(dist, 1).astype(np.float64)
    large = max_exact + (np.log(d / max_exact) / math.log(MAX_DISTANCE / max_exact) * (N_BUCKETS - max_exact)).astype(np.int32)
    large = np.minimum(large, N_BUCKETS - 1)
    return np.where(dist < max_exact, dist, large).astype(np.int32)


def _bucket_tables():
    return jnp.asarray(np.stack([_bucket_table(r) for _, r in DILATED_PATTERNS]))


def _bias_table_call(rel_bias, buckets):
    def body(rb_ref, bk_ref, o_ref):
        for h in range(N_HEADS):
            bk = bk_ref[h // HEADS_PER_GROUP]

            def step(b, acc):
                return jnp.where(bk == b, rb_ref[b, h], acc)

            o_ref[h] = lax.fori_loop(0, N_BUCKETS, step, jnp.zeros((SPAN, 2 * SPAN), F32))

    return pl.pallas_call(
        body, name="rel_bias_table",
        in_specs=[pl.BlockSpec(memory_space=pltpu.SMEM), pl.BlockSpec(memory_space=pltpu.VMEM)],
        out_specs=pl.BlockSpec(memory_space=pltpu.VMEM),
        out_shape=jax.ShapeDtypeStruct((N_HEADS, SPAN, 2 * SPAN), F32),
    )(rel_bias, buckets)


def _bias_grad_call(dbias, buckets):
    def body(db_ref, bk_ref, o_ref, rows_ref):
        for h in range(N_HEADS):
            bk = bk_ref[h // HEADS_PER_GROUP]
            dv = db_ref[h]

            def step(b, carry):
                rows_ref[h, b] = jnp.sum(jnp.where(bk == b, dv, 0.0), axis=0, keepdims=True)
                return carry

            lax.fori_loop(0, N_BUCKETS, step, 0)
        o_ref[...] = jnp.sum(rows_ref[...], axis=-1, keepdims=True)

    out = pl.pallas_call(
        body, name="rel_bias_grad",
        in_specs=[pl.BlockSpec(memory_space=pltpu.VMEM), pl.BlockSpec(memory_space=pltpu.VMEM)],
        out_specs=pl.BlockSpec(memory_space=pltpu.VMEM),
        out_shape=jax.ShapeDtypeStruct((N_HEADS, N_BUCKETS, 1, 1), F32),
        scratch_shapes=[pltpu.VMEM((N_HEADS, N_BUCKETS, 1, 2 * SPAN), F32)],
    )(dbias, buckets)
    return out.reshape(N_HEADS, N_BUCKETS).T


def _dot_nt(a, b):
    return lax.dot_general(a, b, (((1,), (1,)), ((), ())), preferred_element_type=F32)


def _dot_nn(a, b):
    return lax.dot_general(a, b, (((1,), (0,)), ((), ())), preferred_element_type=F32)


def _dot_tn(a, b):
    return lax.dot_general(a, b, (((0,), (0,)), ((), ())), preferred_element_type=F32)


def _band_masks(n, nb):
    qi = lax.broadcasted_iota(jnp.int32, (SPAN, SPAN), 0)
    ki = lax.broadcasted_iota(jnp.int32, (SPAN, SPAN), 1)
    prev_ok = jnp.logical_and(ki >= qi, n > 0)
    cur_ok = ki <= qi
    next_ok = jnp.logical_and(ki >= qi, n < nb - 1)
    return prev_ok, cur_ok, next_ok


def _wide_band_mask(n):
    qi = lax.broadcasted_iota(jnp.int32, (SPAN, 2 * SPAN), 0)
    ki = lax.broadcasted_iota(jnp.int32, (SPAN, 2 * SPAN), 1)
    prev_ok = jnp.logical_and(jnp.logical_and(ki < SPAN, ki >= qi), n > 0)
    cur_ok = jnp.logical_and(ki >= SPAN, ki - SPAN <= qi)
    return jnp.logical_or(prev_ok, cur_ok)


def _attn_plan(S, group):
    r = DILATED_PATTERNS[group][1]
    hp, per = (HEADS_PER_GROUP, 1) if r == 1 else (2, 4)
    return r, S // (r * SPAN), hp, per


def _residue_rows(rho, r):
    return slice(None) if r == 1 else pl.ds(rho, SPAN, stride=r)


def _for_residues(r, per, fn):
    if r == per:
        for u in range(per):
            fn(u)
        return

    def step(i, carry):
        for u in range(per):
            fn(i * per + u)
        return carry

    lax.fori_loop(0, r // per, step, 0)


def _attn_fwd_call(proj, bias, group, merge_with=None):
    S = proj.shape[0]
    r, nb, hp, per = _attn_plan(S, group)
    scale = HEAD_DIM ** -0.5
    kinds = ("q", "kp", "kc", "vp", "vc") if nb > 1 else ("q", "kc", "vc")
    merge = merge_with is not None
    assert not merge or (r == 1 and hp == HEADS_PER_GROUP and len(merge_with) == 4)

    per_kind = _refs_per_kind(r, hp)

    def body(*refs):
        ins = {kind: refs[i * per_kind:(i + 1) * per_kind] for i, kind in enumerate(kinds)}
        b_ref, *rest = refs[len(kinds) * per_kind:]
        if merge:
            o2_ref, s2_ref, o3_ref, s3_ref, a_ref, ab_ref, lse_ref = rest
        else:
            o_ref, lse_ref = rest
        n = pl.program_id(1)
        prev_ok, cur_ok, _ = _band_masks(n, nb)

        band_ok = _wide_band_mask(n) if nb > 1 else cur_ok

        def residue(rho):
            rows = _residue_rows(rho, r)
            for j in range(hp):
                get = lambda kind: _head_rows(ins[kind], j, rows, r).astype(BF16)
                q = get("q")
                if nb > 1:
                    keys, vals, bias_j = jnp.concatenate([get("kp"), get("kc")], axis=0), jnp.concatenate([get("vp"), get("vc")], axis=0), b_ref[j]
                else:
                    keys, vals, bias_j = get("kc"), get("vc"), b_ref[j, :, SPAN:]
                s = jnp.where(band_ok, _dot_nt(q, keys) * scale + bias_j, NEG_INF)
                m = jnp.max(s, axis=-1, keepdims=True)
                p = jnp.exp(s - m)
                den = jnp.sum(p, axis=-1, keepdims=True)
                o1 = _dot_nn(p.astype(BF16), vals) / den
                s1 = jnp.broadcast_to(m + jnp.log(den), (SPAN, HEAD_DIM))
                if not merge:
                    o_ref[j, rows, :] = o1
                    lse_ref[j, rows, :] = s1
                    continue
                sl = slice(j * HEAD_DIM, (j + 1) * HEAD_DIM)
                s2, s3 = s2_ref[j], s3_ref[j]
                mx = jnp.maximum(jnp.maximum(s1, s2), s3)
                w1 = jnp.exp(s1 - mx)
                w2 = jnp.exp(s2 - mx)
                w3 = jnp.exp(s3 - mx)
                total = w1 + w2 + w3
                merged = (w1 * o1 + w2 * o2_ref[j] + w3 * o3_ref[j]) / total
                a_ref[:, sl] = merged
                ab_ref[:, sl] = merged.astype(BF16)
                lse_ref[:, sl] = mx + jnp.log(total)

        _for_residues(r, per, residue)

    in_specs = [_head_spec(r, nb, hp, kind, group, jj) for kind in kinds for jj in range(per_kind)]
    in_specs.append(pl.BlockSpec((hp, SPAN, 2 * SPAN), lambda j, n: (group * (HEADS_PER_GROUP // hp) + j, 0, 0)))
    out = pl.BlockSpec((hp, r * SPAN, HEAD_DIM), lambda j, n: (j, n, 0))
    operands = [proj] * (len(in_specs) - 1) + [bias]
    if merge:
        in_specs += [out] * 4
        operands += list(merge_with)
        rows = pl.BlockSpec((SPAN, GROUP_WIDTH), lambda j, n: (n, 0))
        out_specs = [rows] * 3
        out_shape = [jax.ShapeDtypeStruct((S, GROUP_WIDTH), dt) for dt in (F32, BF16, F32)]
    else:
        out_specs = [out] * 2
        out_shape = [jax.ShapeDtypeStruct((HEADS_PER_GROUP, S, HEAD_DIM), F32)] * 2
    return pl.pallas_call(
        body, name=f"attn_fwd_g{group}", grid=(HEADS_PER_GROUP // hp, nb),
        in_specs=in_specs, out_specs=out_specs, out_shape=out_shape,
        compiler_params=_params(("parallel", "parallel"), VMEM_LIMIT),
    )(*operands)


_PROJ_PART = dict(q=0, qn=0, kp=1, kc=1, vp=2, vc=2)


def _refs_per_kind(r, hp):
    return 1 if r == 1 else hp


def _head_rows(refs, j, rows, r):
    return refs[0][:, j * HEAD_DIM:(j + 1) * HEAD_DIM] if r == 1 else refs[j][rows, :]


def _head_spec(r, nb, hp, kind, group, jj):
    if kind in _PROJ_PART:
        base = (_PROJ_PART[kind] * N_GROUPS + group) * HEADS_PER_GROUP
    else:
        base = 0
    if kind.endswith("p"):
        row = lambda n: jnp.maximum(n - 1, 0)
    elif kind.endswith("n"):
        row = lambda n: jnp.minimum(n + 1, nb - 1)
    else:
        row = lambda n: n
    if r == 1:
        return pl.BlockSpec((SPAN, hp * HEAD_DIM), lambda j, n: (row(n), base // hp + j))
    return pl.BlockSpec((r * SPAN, HEAD_DIM), lambda j, n: (row(n), base + j * hp + jj))


def _attn_bwd_call(proj, bias, a, da, lse, group, dproj):
    S = proj.shape[0]
    r, nb, hp, per = _attn_plan(S, group)
    scale = HEAD_DIM ** -0.5
    kinds = ("q", "qn", "kp", "kc", "vp", "vc", "da", "dan", "lse", "lsen", "a", "an") if nb > 1 else ("q", "kc", "vc", "da", "lse", "a")
    source = dict(da=da, dan=da, lse=lse, lsen=lse, a=a, an=a)

    per_kind = _refs_per_kind(r, hp)
    per_group = HEADS_PER_GROUP // hp
    block_rows = r * SPAN

    def body(*refs):
        ins = {kind: refs[i * per_kind:(i + 1) * per_kind] for i, kind in enumerate(kinds)}
        b_ref, _, dproj_ref, db_ref, stage_ref, sems = refs[len(kinds) * per_kind:][:6]
        strided_ref = None if r == 1 else refs[-1]
        jg, n = pl.program_id(0), pl.program_id(1)
        prev_ok, cur_ok, next_ok = _band_masks(n, nb)

        @pl.when(n == 0)
        def _():
            db_ref[...] = jnp.zeros_like(db_ref)

        band_ok = _wide_band_mask(n) if nb > 1 else cur_ok

        def fill(slot):
            def put(part, j, rows, value):
                if r == 1:
                    stage_ref[slot, part, :, j * HEAD_DIM:(j + 1) * HEAD_DIM] = value.astype(BF16)
                else:
                    strided_ref[part, j, rows, :] = value

            _for_residues(r, per, functools.partial(residue, put))
            if r > 1:
                for part in range(3):
                    for j in range(hp):
                        for t0 in range(0, block_rows, ROW_TILE):
                            stage_ref[slot, part, t0:t0 + ROW_TILE, j * HEAD_DIM:(j + 1) * HEAD_DIM] = (
                                strided_ref[part, j, t0:t0 + ROW_TILE, :].astype(BF16))

        def residue(put, rho):
            rows = _residue_rows(rho, r)
            for j in range(hp):
                get = lambda kind: _head_rows(ins[kind], j, rows, r)
                q = get("q").astype(BF16)
                kc = get("kc").astype(BF16)
                vc = get("vc").astype(BF16)
                da_q = get("da")
                dav = da_q.astype(BF16)
                lse_q = get("lse")
                dl_q = jnp.sum(get("a") * da_q, axis=-1, keepdims=True)
                if nb == 1:
                    pc = jnp.exp(jnp.where(cur_ok, _dot_nt(q, kc) * scale + b_ref[j, :, SPAN:], NEG_INF) - lse_q)
                    dsc = pc * (_dot_nt(dav, vc) - dl_q)
                    dsc_b = dsc.astype(BF16)
                    dq = _dot_nn(dsc_b, kc)
                    dk = _dot_tn(dsc_b, q)
                    dv = _dot_tn(pc.astype(BF16), dav)
                    db_ref[j, :, SPAN:] += dsc
                else:
                    qn = get("qn").astype(BF16)
                    da_n = get("dan")
                    dan = da_n.astype(BF16)
                    keys = jnp.concatenate([get("kp").astype(BF16), kc], axis=0)
                    vals = jnp.concatenate([get("vp").astype(BF16), vc], axis=0)
                    wide = lambda t: jnp.concatenate([t, t], axis=1)
                    p = jnp.exp(jnp.where(band_ok, _dot_nt(q, keys) * scale + b_ref[j], NEG_INF) - wide(lse_q))
                    ds = p * (_dot_nt(dav, vals) - dl_q)
                    dq = _dot_nn(ds.astype(BF16), keys)
                    db_ref[j] += ds
                    pn = jnp.exp(jnp.where(next_ok, _dot_nt(qn, kc) * scale + b_ref[j, :, :SPAN], NEG_INF) - get("lsen"))
                    dsn = pn * (_dot_nt(dan, vc) - jnp.sum(get("an") * da_n, axis=-1, keepdims=True))
                    both = lambda cur_part, next_part: jnp.concatenate([cur_part.astype(BF16), next_part.astype(BF16)], axis=0)
                    dk = _dot_tn(both(ds[:, SPAN:], dsn), jnp.concatenate([q, qn], axis=0))
                    dv = _dot_tn(both(p[:, SPAN:], pn), jnp.concatenate([dav, dan], axis=0))
                put(0, j, rows, dq * scale)
                put(1, j, rows, dk * scale)
                put(2, j, rows, dv)

        cols = [(part * N_GROUPS + group) * GROUP_WIDTH + jg * (hp * HEAD_DIM) for part in range(3)]
        rows = pl.ds(pl.multiple_of(n * block_rows, SPAN), block_rows)
        _staged_window_stores(stage_ref, dproj_ref, sems, jg * nb + n, per_group * nb, rows, cols, fill)

    band = (hp, SPAN, 2 * SPAN)
    in_specs = [_head_spec(r, nb, hp, kind, group, jj) for kind in kinds for jj in range(per_kind)]
    in_specs += [pl.BlockSpec(band, lambda j, n: (group * per_group + j, 0, 0)), ANY]
    operands = [source.get(kind, proj) for kind in kinds for _ in range(per_kind)] + [bias, dproj]
    scratch = [pltpu.VMEM((2, 3, block_rows, hp * HEAD_DIM), BF16), pltpu.SemaphoreType.DMA((2, 3))]
    if r > 1:
        scratch.append(pltpu.VMEM((3, hp, block_rows, HEAD_DIM), F32))
    return pl.pallas_call(
        body, name=f"attn_bwd_g{group}", grid=(per_group, nb),
        in_specs=in_specs,
        out_specs=[ANY, pl.BlockSpec(band, lambda j, n: (j, 0, 0))],
        out_shape=[jax.ShapeDtypeStruct(dproj.shape, BF16), jax.ShapeDtypeStruct((HEADS_PER_GROUP, SPAN, 2 * SPAN), F32)],
        input_output_aliases={len(operands) - 1: 0},
        scratch_shapes=scratch,
        compiler_params=_params(("arbitrary", "arbitrary"), VMEM_LIMIT),
    )(*operands)


def _tap_rows(xpad_ref, t0, k, width, pad):
    return xpad_ref[pl.ds(t0 + (pad - (width - 1 - k)), TIME_BLOCK), :]


def _conv_block(xpad_ref, t0, w_ref, width, pad):
    acc = None
    for k in range(width):
        term = w_ref[k:k + 1, :] * _tap_rows(xpad_ref, t0, k, width, pad)
        acc = term if acc is None else acc + term
    return acc


def _conv_transpose_block(dpad_ref, t0, w_ref, width):
    acc = None
    for k in range(width):
        term = w_ref[k:k + 1, :] * dpad_ref[pl.ds(t0 + (width - 1 - k), TIME_BLOCK), :]
        acc = term if acc is None else acc + term
    return acc


def _conv_weight_grad(xpad_ref, t0, dy, dw_ref, width, pad):
    for k in range(width):
        dw_ref[k:k + 1, :] += jnp.sum(dy * _tap_rows(xpad_ref, t0, k, width, pad), axis=0, keepdims=True)


def _time_loop(S, step, skip_first=0, skip_last=0):
    def it(tb, carry):
        step(pl.multiple_of(tb * TIME_BLOCK, TIME_BLOCK))
        return carry

    lax.fori_loop(skip_first, S // TIME_BLOCK - skip_last, it, 0)


def _fill_head(head_ref, x_ref, pad):
    head_ref[0:pad, :] = jnp.zeros((pad, LANES), F32)
    head_ref[pad:, :] = x_ref[0:TIME_BLOCK, :]


def _fill_tail(tail_ref, x_ref, pad):
    S = x_ref.shape[0]
    tail_ref[0:TIME_BLOCK, :] = x_ref[S - TIME_BLOCK:S, :]
    tail_ref[TIME_BLOCK:, :] = jnp.zeros((pad, LANES), F32)


def _conv_fwd_call(proj, col0, w, b):
    S = proj.shape[0]
    C = w.shape[1]
    nt = C // LANES
    v0, g0 = col0 // LANES, (col0 + C) // LANES

    def body(val_ref, gate_ref, w_ref, b_ref, o_ref, pad_ref):
        pad_ref[0:CONV_PAD, :] = jnp.zeros((CONV_PAD, LANES), F32)
        pad_ref[CONV_PAD:, :] = val_ref[...] * _sigmoid(gate_ref[...])

        def step(t0):
            o_ref[pl.ds(t0, TIME_BLOCK), :] = _conv_block(pad_ref, t0, w_ref, CONV_WIDTH, CONV_PAD) + b_ref[...]

        _time_loop(S, step)

    seq = lambda off: pl.BlockSpec((S, LANES), lambda i: (0, off + i))
    return pl.pallas_call(
        body, name="conv_module", grid=(nt,),
        in_specs=[seq(v0), seq(g0), pl.BlockSpec((CONV_WIDTH, LANES), lambda i: (0, i)), pl.BlockSpec((1, LANES), lambda i: (0, i))],
        out_specs=seq(0), out_shape=jax.ShapeDtypeStruct((S, C), F32),
        scratch_shapes=[pltpu.VMEM((S + CONV_PAD, LANES), F32)],
        compiler_params=_params(("parallel",)),
    )(proj, proj, w, b)


def _conv_bwd_call(proj, col0, w, dc1, dproj):
    S = proj.shape[0]
    C = w.shape[1]
    nt = C // LANES
    v0, g0 = col0 // LANES, (col0 + C) // LANES

    def body(val_ref, gate_ref, w_ref, dy_ref, _, dproj_ref, dw_ref, db_ref, xpad_ref, tail_ref, dwacc_ref, stage_ref, sems):
        i = pl.program_id(0)
        xpad_ref[0:CONV_PAD, :] = jnp.zeros((CONV_PAD, LANES), F32)
        xpad_ref[CONV_PAD:, :] = val_ref[...] * _sigmoid(gate_ref[...])
        _fill_tail(tail_ref, dy_ref, CONV_PAD)
        dwacc_ref[...] = jnp.zeros_like(dwacc_ref)

        def fill(slot):
            def block(t0, dy_src, dy_t0):
                rows = pl.ds(t0, TIME_BLOCK)
                _conv_weight_grad(xpad_ref, t0, dy_ref[rows, :], dwacc_ref, CONV_WIDTH, CONV_PAD)
                dc0 = _conv_transpose_block(dy_src, dy_t0, w_ref, CONV_WIDTH)
                sg = _sigmoid(gate_ref[rows, :])
                stage_ref[slot, 0, rows, :] = (dc0 * sg).astype(BF16)
                stage_ref[slot, 1, rows, :] = (dc0 * val_ref[rows, :] * sg * (1.0 - sg)).astype(BF16)

            _time_loop(S, lambda t0: block(t0, dy_ref, t0), skip_last=1)
            block(S - TIME_BLOCK, tail_ref, 0)

        _staged_window_stores(stage_ref, dproj_ref, sems, i, nt, pl.ds(0, S),
                              [col0 + i * LANES, col0 + C + i * LANES], fill)
        dw_ref[...] = dwacc_ref[...]
        db_ref[...] = jnp.sum(dy_ref[...], axis=0, keepdims=True)

    seq = lambda off: pl.BlockSpec((S, LANES), lambda i: (0, off + i))
    return pl.pallas_call(
        body, name="conv_module_bwd", grid=(nt,),
        in_specs=[seq(v0), seq(g0), pl.BlockSpec((CONV_WIDTH, LANES), lambda i: (0, i)), seq(0), ANY],
        out_specs=[ANY, pl.BlockSpec((CONV_PAD, LANES), lambda i: (0, i)), pl.BlockSpec((1, LANES), lambda i: (0, i))],
        out_shape=[jax.ShapeDtypeStruct(dproj.shape, BF16),
                   jax.ShapeDtypeStruct((CONV_PAD, C), F32), jax.ShapeDtypeStruct((1, C), F32)],
        input_output_aliases={4: 0},
        scratch_shapes=[pltpu.VMEM((S + CONV_PAD, LANES), F32), pltpu.VMEM((TIME_BLOCK + CONV_PAD, LANES), F32),
                        pltpu.VMEM((CONV_PAD, LANES), F32),
                        pltpu.VMEM((2, 2, S, LANES), BF16), pltpu.SemaphoreType.DMA((2, 2))],
        compiler_params=_params(("arbitrary",)),
    )(proj, proj, w, dc1, dproj)


def _ffn_fwd_call(u, w, b):
    S, C2 = u.shape
    C = C2 // 2
    width = _pick(C, (2 * LANES, LANES))
    nt = C // width

    def body(ug_ref, uv_ref, wg_ref, wv_ref, bg_ref, bv_ref, f_ref, xg_ref, xv_ref):
        zeros = jnp.zeros((FFN_PAD, LANES), F32)
        for part in range(width // LANES):
            lanes = slice(part * LANES, (part + 1) * LANES)
            xg_ref[0:FFN_PAD, :] = zeros
            xv_ref[0:FFN_PAD, :] = zeros
            xg_ref[FFN_PAD:, :] = ug_ref[:, lanes]
            xv_ref[FFN_PAD:, :] = uv_ref[:, lanes]

            def step(t0, lanes=lanes):
                cg = _conv_block(xg_ref, t0, wg_ref.at[:, lanes], FFN_CONV_WIDTH, FFN_PAD) + bg_ref[:, lanes]
                cv = _conv_block(xv_ref, t0, wv_ref.at[:, lanes], FFN_CONV_WIDTH, FFN_PAD) + bv_ref[:, lanes]
                f_ref[pl.ds(t0, TIME_BLOCK), lanes] = (_gelu(cg) * cv).astype(BF16)

            _time_loop(S, step)

    seq = lambda off: pl.BlockSpec((S, width), lambda i: (0, off + i))
    wsp = lambda off: pl.BlockSpec((FFN_CONV_WIDTH, width), lambda i: (0, off + i))
    bsp = lambda off: pl.BlockSpec((1, width), lambda i: (0, off + i))
    return pl.pallas_call(
        body, name="ffn_conv_geglu", grid=(nt,),
        in_specs=[seq(0), seq(nt), wsp(0), wsp(nt), bsp(0), bsp(nt)],
        out_specs=seq(0), out_shape=jax.ShapeDtypeStruct((S, C), BF16),
        scratch_shapes=[pltpu.VMEM((FFN_PAD + S, LANES), F32)] * 2,
        compiler_params=_params(("parallel",)),
    )(u, u, w, w, b, b)


def _ffn_bwd_call(u, w, b, df):
    S, C2 = u.shape
    C = C2 // 2
    nt = C // LANES

    def body(ug_ref, uv_ref, wg_ref, wv_ref, bg_ref, bv_ref, df_ref,
             du_ref, dwg_ref, dwv_ref, dbg_ref, dbv_ref,
             hg_ref, hv_ref, dg_ref, dv_ref, dwg_acc, dwv_acc, dbg_acc, dbv_acc):
        zeros = jnp.zeros((FFN_PAD, LANES), F32)
        _fill_head(hg_ref, ug_ref, FFN_PAD)
        _fill_head(hv_ref, uv_ref, FFN_PAD)
        dg_ref[S:, :] = zeros
        dv_ref[S:, :] = zeros
        dwg_acc[...] = jnp.zeros_like(dwg_acc)
        dwv_acc[...] = jnp.zeros_like(dwv_acc)
        dbg_acc[...] = jnp.zeros_like(dbg_acc)
        dbv_acc[...] = jnp.zeros_like(dbv_acc)

        def first(t0, xg_ref, xv_ref, x_t0, pad):
            rows = pl.ds(t0, TIME_BLOCK)
            cg = _conv_block(xg_ref, x_t0, wg_ref, FFN_CONV_WIDTH, pad) + bg_ref[...]
            cv = _conv_block(xv_ref, x_t0, wv_ref, FFN_CONV_WIDTH, pad) + bv_ref[...]
            dfb = df_ref[rows, :]
            gelu, gelu_grad = _gelu_and_grad(cg)
            dcg = dfb * cv * gelu_grad
            dcv = dfb * gelu
            dg_ref[rows, :] = dcg
            dv_ref[rows, :] = dcv
            _conv_weight_grad(xg_ref, x_t0, dcg, dwg_acc, FFN_CONV_WIDTH, pad)
            _conv_weight_grad(xv_ref, x_t0, dcv, dwv_acc, FFN_CONV_WIDTH, pad)
            dbg_acc[...] += jnp.sum(dcg, axis=0, keepdims=True)
            dbv_acc[...] += jnp.sum(dcv, axis=0, keepdims=True)

        def second(t0):
            rows = pl.ds(t0, TIME_BLOCK)
            du_ref[0, rows, :] = _conv_transpose_block(dg_ref, t0, wg_ref, FFN_CONV_WIDTH).astype(BF16)
            du_ref[1, rows, :] = _conv_transpose_block(dv_ref, t0, wv_ref, FFN_CONV_WIDTH).astype(BF16)

        first(0, hg_ref, hv_ref, 0, FFN_PAD)
        _time_loop(S, lambda t0: first(t0, ug_ref, uv_ref, t0, 0), skip_first=1)
        _time_loop(S, second)
        dwg_ref[...] = dwg_acc[...]
        dwv_ref[...] = dwv_acc[...]
        dbg_ref[...] = dbg_acc[...]
        dbv_ref[...] = dbv_acc[...]

    seq = lambda off: pl.BlockSpec((S, LANES), lambda i: (0, off + i))
    wsp = lambda off: pl.BlockSpec((FFN_CONV_WIDTH, LANES), lambda i: (0, off + i))
    bsp = lambda off: pl.BlockSpec((1, LANES), lambda i: (0, off + i))
    return pl.pallas_call(
        body, name="ffn_conv_geglu_bwd", grid=(nt,),
        in_specs=[seq(0), seq(nt), wsp(0), wsp(nt), bsp(0), bsp(nt), seq(0)],
        out_specs=[pl.BlockSpec((2, S, LANES), lambda i: (0, 0, i)),
                   pl.BlockSpec((SUBLANES, LANES), lambda i: (0, i)), pl.BlockSpec((SUBLANES, LANES), lambda i: (0, i)),
                   bsp(0), bsp(0)],
        out_shape=[jax.ShapeDtypeStruct((2, S, C), BF16)] + [jax.ShapeDtypeStruct((SUBLANES, C), F32)] * 2
        + [jax.ShapeDtypeStruct((1, C), F32)] * 2,
        scratch_shapes=[pltpu.VMEM((FFN_PAD + TIME_BLOCK, LANES), F32)] * 2 + [pltpu.VMEM((S + FFN_PAD, LANES), F32)] * 2
        + [pltpu.VMEM((SUBLANES, LANES), F32)] * 2
        + [pltpu.VMEM((1, LANES), F32)] * 2,
        compiler_params=_params(("parallel",)),
    )(u, u, w, w, b, b, df)


def _adamw(w_ref, g_ref, m_ref, v_ref, d_ref, mo_ref, vo_ref):
    gv = g_ref[...]
    mn = ADAM_B1 * m_ref[...] + (1.0 - ADAM_B1) * gv
    vn = ADAM_B2 * v_ref[...] + (1.0 - ADAM_B2) * (gv * gv)
    mo_ref[...] = mn
    vo_ref[...] = vn
    m_hat = mn * (1.0 / (1.0 - ADAM_B1 ** ADAM_STEP))
    v_hat = vn * (1.0 / (1.0 - ADAM_B2 ** ADAM_STEP))
    d_ref[...] = -ADAM_LR * (m_hat / (jnp.sqrt(v_hat) + ADAM_EPS) + ADAM_WD * w_ref[...])


def _adamw_call(w, g, m, v, name):
    R, C = w.shape
    tr = _row_tile(R, C)

    def body(w_ref, g_ref, m_ref, v_ref, go_ref, d_ref, mo_ref, vo_ref):
        go_ref[...] = g_ref[...]
        _adamw(w_ref, g_ref, m_ref, v_ref, d_ref, mo_ref, vo_ref)

    spec = pl.BlockSpec((tr, C), lambda i: (i, 0))
    return pl.pallas_call(
        body, name=name, grid=(R // tr,),
        in_specs=[spec] * 4, out_specs=[spec] * 4,
        out_shape=[jax.ShapeDtypeStruct((R, C), F32)] * 4,
        compiler_params=_params(("parallel",)),
    )(w, g, m, v)


def _adamw_small_call(ws, gs, ms, vs):
    n = len(ws)

    def body(*refs):
        w_refs, g_refs, m_refs, v_refs, d_refs, mo_refs, vo_refs = (refs[i * n:(i + 1) * n] for i in range(7))
        for i in range(n):
            _adamw(w_refs[i], g_refs[i], m_refs[i], v_refs[i], d_refs[i], mo_refs[i], vo_refs[i])

    whole = pl.BlockSpec(memory_space=pltpu.VMEM)
    outs = pl.pallas_call(
        body, name="adamw_small",
        in_specs=[whole] * (4 * n), out_specs=[whole] * (3 * n),
        out_shape=[jax.ShapeDtypeStruct(w.shape, F32) for w in ws] * 3,
    )(*ws, *gs, *ms, *vs)
    return outs[:n], outs[n:2 * n], outs[2 * n:]


def _position():
    return lax.axis_index("x"), lax.axis_index("y"), lax.axis_index("c")


def _chip_peers(x, y):
    return [(x, 1 - y), (1 - x, y), (1 - x, 1 - y)]


def _half_rows(ref, core, rows):
    h = rows // 2
    start = pl.multiple_of(core * h, PACKED_ROWS)
    return ref.at[pl.ds(start, h), :] if len(ref.shape) == 2 else ref.at[:, pl.ds(start, h), :]


def _shard_half(ref, shard, core, rows):
    h = rows // 2
    return ref.at[shard, pl.ds(pl.multiple_of(core * h, PACKED_ROWS), h), :]


ANY = pl.BlockSpec(memory_space=pl.ANY)


def _first_hop_copies(srcs, lands):
    x, y, c = _position()
    chip = 2 * x + y
    targets = [(px, py, c) for px, py in _chip_peers(x, y)] + [(x, y, 1 - c)]
    rows = srcs[0].shape[0]
    out = []
    for i, (s, l) in enumerate(zip(srcs, lands)):
        for k, dev in enumerate(targets):
            if i == 0 and k < 3:
                out.append((_half_rows(s, c, rows), _shard_half(l, chip, c, rows), dev, k))
            else:
                out.append((s, l.at[chip], dev, len(targets) * i + k))
    return out


def _second_hop_copies(srcs, lands):
    x, y, c = _position()
    rows = lands[0].shape[1]
    out = []
    for k, (px, py) in enumerate(_chip_peers(x, y)):
        half = _shard_half(lands[0], 2 * px + py, c, rows)
        out.append((half, half, (x, y, 1 - c), k))
    return out


HBM_SPEC = pl.BlockSpec(memory_space=pltpu.HBM)
SEM_SPEC = pl.BlockSpec(memory_space=pltpu.SEMAPHORE)
DATAFLOW = pltpu.SideEffectType.DATAFLOW_SIDE_EFFECTING


def _in_hbm(a):
    return pltpu.with_memory_space_constraint(a, pltpu.HBM)


SIBLING_HANDSHAKE_IDS = dict(grad_exchange_start_w_up=1, grad_assemble_start_others=2, grad_assemble_start_w_in=3)


def _split_start(name, groups, after, carry=None, sibling_only=False):
    spans, arrays = [], []
    for srcs, lands, _, _ in groups:
        spans.append((len(arrays), len(srcs), len(lands)))
        arrays += list(srcs) + list(lands)
    if carry is not None:
        arrays.append(carry)
    na, ng = len(arrays), len(groups)

    def body(*refs):
        sems, token = refs[na + 1:na + 1 + 2 * ng], refs[-1]
        if sibling_only:
            x, y, c = _position()
            barrier = pltpu.get_barrier_semaphore()
            pl.semaphore_signal(barrier, inc=1, device_id=(x, y, 1 - c), device_id_type=MESH)
            pl.semaphore_wait(barrier, 1)
        for g, (_, _, _, copies) in enumerate(groups):
            off, ns, nl = spans[g]
            for src, dst, dev, idx in copies(refs[off:off + ns], refs[off + ns:off + ns + nl]):
                pltpu.make_async_remote_copy(src_ref=src, dst_ref=dst, send_sem=sems[2 * g].at[idx], recv_sem=sems[2 * g + 1].at[idx],
                                             device_id=dev, device_id_type=MESH).start()
        token[...] = jnp.zeros_like(token)

    outs = pl.pallas_call(
        body, name=name,
        in_specs=[HBM_SPEC] * na + [ANY],
        out_specs=[SEM_SPEC] * (2 * ng) + [HBM_SPEC] * na + [pl.BlockSpec(memory_space=pltpu.VMEM)],
        out_shape=[pltpu.SemaphoreType.DMA((n_sems,)) for _, _, n_sems, _ in groups for _ in range(2)]
        + [pltpu.HBM(a.shape, a.dtype) for a in arrays] + [jax.ShapeDtypeStruct((SUBLANES, LANES), F32)],
        input_output_aliases={i: 2 * ng + i for i in range(na)},
        compiler_params=pltpu.CompilerParams(has_side_effects=DATAFLOW,
                                             collective_id=SIBLING_HANDSHAKE_IDS[name] if sibling_only else None),
    )(*[_in_hbm(a) for a in arrays], after)
    started = []
    for g, (off, ns, nl) in enumerate(spans):
        thru = outs[2 * ng + off:2 * ng + off + ns + nl]
        started.append(dict(send=outs[2 * g], recv=outs[2 * g + 1], srcs=list(thru[:ns]), lands=list(thru[ns:]),
                            tile=outs[-1], token=outs[-1][0, 0], carry=None if carry is None else outs[2 * ng + na - 1]))
    return started


def _split_wait(name, started, copies, after):
    n, m = len(started["srcs"]), len(started["lands"])
    after = list(after) if isinstance(after, (list, tuple)) else [after]

    def body(*refs):
        src_refs, land_refs = refs[:n], refs[n:n + m]
        send_sem, recv_sem = refs[n + m], refs[n + m + 1]
        for src, dst, dev, idx in copies(src_refs, land_refs):
            cp = pltpu.make_async_remote_copy(src_ref=src, dst_ref=dst, send_sem=send_sem.at[idx], recv_sem=recv_sem.at[idx],
                                              device_id=dev, device_id_type=MESH)
            cp.wait_send()
            cp.wait_recv()

    arrays = started["srcs"] + started["lands"]
    outs = pl.pallas_call(
        body, name=name,
        in_specs=[HBM_SPEC] * (n + m) + [SEM_SPEC, SEM_SPEC] + [ANY] * len(after),
        out_specs=[HBM_SPEC] * (n + m),
        out_shape=[pltpu.HBM(a.shape, a.dtype) for a in arrays],
        input_output_aliases={i: i for i in range(n + m)},
        compiler_params=pltpu.CompilerParams(has_side_effects=DATAFLOW),
    )(*arrays, started["send"], started["recv"], *after)
    return list(outs)


def _gather_copies(srcs, lands):
    x, y, c = _position()
    chip = 2 * x + y
    targets = [(px, py, c) for px, py in _chip_peers(x, y)] + [(x, y, 1 - c)]
    return [(s, l.at[chip], dev, len(targets) * i + k) for i, (s, l) in enumerate(zip(srcs, lands)) for k, dev in enumerate(targets)]


def _sibling_copies(srcs, lands):
    x, y, c = _position()
    return [(_half_rows(srcs[0], 1 - c, srcs[0].shape[1]), lands[0], (x, y, 1 - c), 0)]


def _sibling_whole_copies(srcs, lands):
    x, y, c = _position()
    return [(srcs[0], lands[0], (x, y, 1 - c), 0)]


def _exchange_copies(srcs, lands):
    x, y, c = _position()
    return [(srcs[0].at[2 * px + py], lands[0].at[k], (px, py, c), k) for k, (px, py) in enumerate(_chip_peers(x, y))]


def _pair_sum_call(grad, recv, chip_core, name):
    _, h, B = recv.shape
    tr = _row_tile(h, B)

    def body(cc_ref, g_ref, r_ref, o_ref, ob_ref):
        s = g_ref[...] + r_ref[...]
        ob_ref[...] = s.astype(BF16)

        @pl.when(pl.program_id(1) == cc_ref[0])
        def _():
            o_ref[...] = s

    g_spec = pl.BlockSpec((None, tr, B), lambda i, q, cc_ref: (q, cc_ref[1] * (h // tr) + i, 0))
    spec = pl.BlockSpec((None, tr, B), lambda i, q, cc_ref: (q, i, 0))
    own_spec = pl.BlockSpec((tr, B), lambda i, q, cc_ref: (i, 0))
    return pl.pallas_call(
        body, name=name,
        grid_spec=pltpu.PrefetchScalarGridSpec(num_scalar_prefetch=1, grid=(h // tr, N_CHIPS), in_specs=[g_spec, spec],
                                               out_specs=[own_spec, spec]),
        out_shape=[jax.ShapeDtypeStruct((h, B), F32), jax.ShapeDtypeStruct(recv.shape, BF16)],
        compiler_params=_params(("parallel", "arbitrary")),
    )(chip_core, grad, recv)


def _chip_sum_call(partial, recv, chip_core, name):
    _, h, B = recv.shape
    tr = _row_tile(h, B)

    def body(cc_ref, p_ref, r_ref, o_ref):
        o_ref[...] = ((p_ref[...] + r_ref[0].astype(F32)) + r_ref[1].astype(F32)) + r_ref[2].astype(F32)

    return pl.pallas_call(
        body, name=name,
        grid_spec=pltpu.PrefetchScalarGridSpec(
            num_scalar_prefetch=1, grid=(h // tr,),
            in_specs=[pl.BlockSpec((tr, B), lambda i, cc_ref: (i, 0)),
                      pl.BlockSpec((3, tr, B), lambda i, cc_ref: (0, i, 0))],
            out_specs=pl.BlockSpec((tr, B), lambda i, cc_ref: (cc_ref[1] * (h // tr) + i, 0))),
        out_shape=jax.ShapeDtypeStruct((2 * h, B), F32),
        compiler_params=_params(("parallel",)),
    )(chip_core, partial, recv)


def _assemble_copies(srcs, lands):
    x, y, c = _position()
    out = []
    for i, land in enumerate(lands):
        half = _half_rows(land, c, land.shape[0])
        out.append((half, half, (x, y, 1 - c), i))
    return out


N_DEVICES = 8


def _allsum_copies(srcs, lands):
    x, y, c = _position()
    me = 4 * x + 2 * y + c
    out = []
    for k in range(1, N_DEVICES):
        peer = (1 - x if k & 4 else x, 1 - y if k & 2 else y, 1 - c if k & 1 else c)
        out.append((srcs[0], lands[0].at[me], peer, k - 1))
    return out


def _ordered_sum_call(mine, landed, me_chip, shapes, sharded_cols):
    rows = mine.shape[0]
    outs = [(s[0], n) if n else s for s, n in zip(shapes, sharded_cols)]

    def body(mc_ref, x_ref, l_ref, *refs):
        acc_ref = refs[-1]
        acc = jnp.where(mc_ref[0] == 0, x_ref[...], l_ref[0])
        for d in range(1, N_DEVICES):
            acc = acc + jnp.where(mc_ref[0] == d, x_ref[...], l_ref[d])
        acc_ref[...] = acc
        first = 0
        for o_ref, (r, c), n in zip(refs[:-1], shapes, sharded_cols):
            per_row = c // LANES

            def unpack(chip, o_ref=o_ref, r=r, n=n, per_row=per_row, first=first):
                for i in range(r):
                    for j in range((n or per_row * LANES) // LANES):
                        src = first + i * per_row + chip * ((n or 0) // LANES) + j
                        o_ref[i:i + 1, j * LANES:(j + 1) * LANES] = acc_ref[src:src + 1, :]

            if n:
                for q in range(N_CHIPS):
                    pl.when(mc_ref[1] == q)(functools.partial(unpack, q))
            else:
                unpack(0)
            first += r * per_row

    results = pl.pallas_call(
        body, name="small_grad_sum",
        in_specs=[pl.BlockSpec(memory_space=pltpu.SMEM), pl.BlockSpec(memory_space=pltpu.VMEM), pl.BlockSpec(memory_space=pltpu.VMEM)],
        out_specs=[pl.BlockSpec(memory_space=pltpu.VMEM)] * len(outs),
        out_shape=[jax.ShapeDtypeStruct(s, F32) for s in outs],
        scratch_shapes=[pltpu.VMEM((rows, LANES), F32)],
    )(me_chip, mine, landed)
    return results


def _pack(arrays):
    flat = jnp.concatenate([a.reshape(-1).astype(F32) for a in arrays])
    rows = -(-flat.shape[0] // LANES)
    rows = -(-rows // SUBLANES) * SUBLANES
    flat = jnp.pad(flat, (0, rows * LANES - flat.shape[0]))
    return flat.reshape(rows, LANES)


def _local_step(xs, target, P, late_weights, on_grad):
    S, D = xs.shape
    qkv_width = 3 * N_HEADS * HEAD_DIM
    glu_col0, gate_col0 = qkv_width, qkv_width + 2 * D
    shard_major = lambda g: g.reshape(N_CHIPS, g.shape[0] // N_CHIPS, g.shape[1])

    h1 = _rms_fwd_call(xs, P["norm_mix_pre"])
    buckets = _bucket_tables()
    bias = _bias_table_call(P["rel_bias"] + 0.0 * h1[0, 0].astype(F32), buckets)
    P = dict(P, **late_weights("in", bias))
    proj = _matmul(h1, P["w_in"], "nn", "proj_in")
    dilated = []
    for g in range(1, N_GROUPS):
        dilated += _attn_fwd_call(proj, bias, g)
    a, a_bf, lse = _attn_fwd_call(proj, bias, 0, merge_with=dilated)
    P = dict(P, **late_weights("mix", a_bf))
    y_a = _matmul(a_bf, P["w_attn_out"], "nn", "attn_out")
    c1 = _conv_fwd_call(proj, glu_col0, P["conv_dw_w"], P["conv_dw_b"])
    cact = _ln_silu_call(c1, P["conv_ln_g"], P["conv_ln_b"])
    y_c = _matmul(cact, P["conv_pw_w"], "nn", "conv_pw")
    mixed = _mix_call(proj, gate_col0, P["b_gate"], y_a, y_c)
    out = _matmul(mixed, P["w_out"], "nn", "mix_out")
    x1, h2 = _res1_call(xs, out, P["norm_mix_post"], P["norm_ffn_pre"])
    P = dict(P, **late_weights("up", h2))
    u = _matmul(h2, P["w_up"], "nn", "ffn_up")
    f = _ffn_fwd_call(u, P["ffn_conv_w"], P["ffn_conv_b"])
    P = dict(P, **late_weights("down", f))
    yff = _matmul(f, P["w_down"], "nn", "ffn_down")
    loss_tile, dx2, dyff, dg_ffn_post = _loss_call(yff, x1, P["norm_ffn_post"], target)

    G = {}
    G["norm_ffn_post"] = dg_ffn_post
    on_grad("w_down", shard_major(_matmul(f, dyff, "tn", "ffn_down_dw")))
    df = _matmul(dyff, P["w_down"], "nt", "ffn_down_dx")
    du, dwg, dwv, dbg, dbv = _ffn_bwd_call(u, P["ffn_conv_w"], P["ffn_conv_b"], df)
    G["ffn_conv_w"] = jnp.concatenate([dwg[:FFN_CONV_WIDTH], dwv[:FFN_CONV_WIDTH]], axis=1)
    G["ffn_conv_b"] = jnp.concatenate([dbg, dbv], axis=1)
    du = on_grad("w_up", functools.partial(_grad_half_matmul, h2, "ffn_up_dw"), carry=du)
    dh2 = _matmul(du, P["w_up"], "nt", "ffn_up_dx")
    dx1, dout, G["norm_ffn_pre"], G["norm_mix_post"] = _mid_bwd_call(x1, P["norm_ffn_pre"], dh2, dx2, out, P["norm_mix_post"])
    on_grad("w_out", shard_major(_matmul(mixed, dout, "tn", "mix_out_dw")))
    dmixed = _matmul(dout, P["w_out"], "nt", "mix_out_dx")
    dya, dyc, dproj, dba, dbc = _mix_bwd_call(dmixed, proj, gate_col0, P["b_gate"], y_a, y_c)
    G["b_gate"] = jnp.concatenate([dba, dbc], axis=1)
    on_grad("w_attn_out", _matmul(a_bf, dya, "tn", "attn_out_dw", out_shards=True))
    dyc = on_grad("conv_pw_w", shard_major(_matmul(cact, dyc, "tn", "conv_pw_dw")), carry=dyc)
    da = _matmul(dya, P["w_attn_out"], "nt", "attn_out_dx")
    dcact = _matmul(dyc, P["conv_pw_w"], "nt", "conv_pw_dx")
    dc1, G["conv_ln_g"], G["conv_ln_b"] = _ln_silu_bwd_call(c1, P["conv_ln_g"], P["conv_ln_b"], dcact)
    dproj, dw_dw, G["conv_dw_b"] = _conv_bwd_call(proj, glu_col0, P["conv_dw_w"], dc1, dproj)
    G["conv_dw_w"] = dw_dw[:CONV_WIDTH]
    dbs = []
    for g in range(N_GROUPS):
        dproj, db = _attn_bwd_call(proj, bias, a, da, lse, g, dproj)
        dbs.append(db)
    G["rel_bias"] = _bias_grad_call(jnp.concatenate(dbs, axis=0), buckets)
    dproj = on_grad("w_in", functools.partial(_grad_half_matmul, h1, "proj_in_dw"), carry=dproj)
    dh1 = _matmul(dproj, P["w_in"], "nt", "proj_in_dx")
    dh1 = on_grad(None, None, carry=dh1)
    grad_x, G["norm_mix_pre"] = _in_bwd_call(xs, P["norm_mix_pre"], dh1, dx1)
    return loss_tile, grad_x, G


def kernel(x, w_in, b_gate, rel_bias, w_attn_out, conv_dw_w, conv_dw_b, conv_ln_g, conv_ln_b, conv_pw_w, w_out, norm_mix_pre, norm_mix_post, norm_ffn_pre, norm_ffn_post, w_up, ffn_conv_w, ffn_conv_b, w_down, loss_target, m_w_in, m_b_gate, m_rel_bias, m_w_attn_out, m_conv_dw_w, m_conv_dw_b, m_conv_ln_g, m_conv_ln_b, m_conv_pw_w, m_w_out, m_norm_mix_pre, m_norm_mix_post, m_norm_ffn_pre, m_norm_ffn_post, m_w_up, m_ffn_conv_w, m_ffn_conv_b, m_w_down, v_w_in, v_b_gate, v_rel_bias, v_w_attn_out, v_conv_dw_w, v_conv_dw_b, v_conv_ln_g, v_conv_ln_b, v_conv_pw_w, v_w_out, v_norm_mix_pre, v_norm_mix_post, v_norm_ffn_pre, v_norm_ffn_post, v_w_up, v_ffn_conv_w, v_ffn_conv_b, v_w_down):
    weights = dict(w_in=w_in, b_gate=b_gate, rel_bias=rel_bias, w_attn_out=w_attn_out, conv_dw_w=conv_dw_w, conv_dw_b=conv_dw_b,
                   conv_ln_g=conv_ln_g, conv_ln_b=conv_ln_b, conv_pw_w=conv_pw_w, w_out=w_out, norm_mix_pre=norm_mix_pre,
                   norm_mix_post=norm_mix_post, norm_ffn_pre=norm_ffn_pre, norm_ffn_post=norm_ffn_post, w_up=w_up,
                   ffn_conv_w=ffn_conv_w, ffn_conv_b=ffn_conv_b, w_down=w_down)
    m_in = dict(w_in=m_w_in, b_gate=m_b_gate, rel_bias=m_rel_bias, w_attn_out=m_w_attn_out, conv_dw_w=m_conv_dw_w,
                conv_dw_b=m_conv_dw_b, conv_ln_g=m_conv_ln_g, conv_ln_b=m_conv_ln_b, conv_pw_w=m_conv_pw_w, w_out=m_w_out,
                norm_mix_pre=m_norm_mix_pre, norm_mix_post=m_norm_mix_post, norm_ffn_pre=m_norm_ffn_pre,
                norm_ffn_post=m_norm_ffn_post, w_up=m_w_up, ffn_conv_w=m_ffn_conv_w, ffn_conv_b=m_ffn_conv_b, w_down=m_w_down)
    v_in = dict(w_in=v_w_in, b_gate=v_b_gate, rel_bias=v_rel_bias, w_attn_out=v_w_attn_out, conv_dw_w=v_conv_dw_w,
                conv_dw_b=v_conv_dw_b, conv_ln_g=v_conv_ln_g, conv_ln_b=v_conv_ln_b, conv_pw_w=v_conv_pw_w, w_out=v_w_out,
                norm_mix_pre=v_norm_mix_pre, norm_mix_post=v_norm_mix_post, norm_ffn_pre=v_norm_ffn_pre,
                norm_ffn_post=v_norm_ffn_post, w_up=v_w_up, ffn_conv_w=v_ffn_conv_w, ffn_conv_b=v_ffn_conv_b, w_down=v_w_down)
    names = list(weights)
    xi, yi, ci = _position()
    chip = 2 * xi + yi
    core_arr = jnp.reshape(ci, (1,)).astype(jnp.int32)

    xs = x[0]
    target = loss_target[0]
    S, D = xs.shape

    big = ["w_in", "w_attn_out", "conv_pw_w", "w_out", "w_up", "w_down"]
    row_sharded = ("conv_pw_w", "w_out", "w_down")
    natural = lambda k, g: g.reshape(-1, g.shape[2]) if k in row_sharded else g
    first_srcs = [w_in[0].astype(BF16), conv_dw_w[0], ffn_conv_w[0]]
    first_lands = [lax.empty((N_CHIPS,) + s.shape, s.dtype) for s in first_srcs]
    (first_hop,) = _split_start("gather_in_start", [(first_srcs, first_lands, 4 * len(first_srcs), _first_hop_copies)], core_arr)
    launched = first_hop["token"]
    late_sets = dict(mix=["w_attn_out", "conv_pw_w", "w_out"], up=["w_up"], down=["w_down"])
    late_groups = []
    for keys in late_sets.values():
        srcs = [(weights[k][0] + launched).astype(BF16) for k in keys]
        late_groups.append((srcs, [lax.empty((N_CHIPS,) + s.shape, BF16) for s in srcs], 4 * len(keys), _gather_copies))
    started = {}

    def late_weights(tag, after):
        if tag == "in":
            casts = [s for srcs, _, _, _ in late_groups for s in srcs]
            w_in_halves, dw4, fc4 = _split_wait("gather_in_wait", first_hop, _first_hop_copies, [after] + casts)[len(first_srcs):]
            second_hop, *late = _split_start("gather_in_pass_start", [([], [w_in_halves], 3, _second_hop_copies)] + late_groups, dw4)
            started.update(zip(late_sets, late))
            (w_in_full,) = _split_wait("gather_in_pass_wait", second_hop, _second_hop_copies, second_hop["tile"])
            return dict(w_in=w_in_full, conv_dw_w=jnp.concatenate(list(dw4), axis=1), ffn_conv_w=jnp.concatenate(list(fc4), axis=1))
        landed = _split_wait(f"gather_{tag}_wait", started[tag], _gather_copies, after)[len(late_sets[tag]):]
        return {k: natural(k, g) for k, g in zip(late_sets[tag], landed)}

    chip_core = jnp.stack([chip, ci]).astype(jnp.int32)
    exchanging, pending, second_half = {}, {}, {}

    held = []

    def launch(tag, after, carry=None):
        keys, groups, partial = [], [], {}
        for k in list(exchanging):
            st, copies = exchanging.pop(k)
            gk, r1 = _split_wait(f"sibling_exchange_wait_{k}", st, copies, after)
            if k in second_half:
                partial[k], s16 = second_half.pop(k)(init=r1)
            else:
                partial[k], s16 = _pair_sum_call(gk, r1, chip_core, f"pair_sum_{k}")
            keys.append(k)
            groups.append(([s16], [lax.empty((3,) + s16.shape[1:], BF16)], 3, _exchange_copies))
        fresh = [(k, copies) for k, _, copies in held]
        for _, g3, copies in held:
            rows = g3.shape[1] // 2 if copies is _sibling_copies else g3.shape[1]
            groups.append(([g3], [lax.empty((N_CHIPS, rows, g3.shape[2]), F32)], 1, copies))
        held.clear()
        begun = _split_start(f"grad_exchange_start_{tag}", groups, core_arr, carry, sibling_only=not keys)
        for k, st in zip(keys, begun):
            pending[k] = (partial[k], st)
        for (k, copies), st in zip(fresh, begun[len(keys):]):
            exchanging[k] = (st, copies)
        return begun[0]["carry"]

    others = [k for k in big if k != "w_in"]
    assembling = {}

    def on_grad(k, g, carry=None):
        if k is None:
            carried = launch("last", carry[:SUBLANES, :LANES], carry)
            assembling["others"] = assemble_start(others, carried, "others", carried)
            return assembling["others"]["carry"]
        if callable(g):
            theirs = g(jnp.stack([chip, 1 - ci]).astype(jnp.int32), carry)
            held.append((k, theirs, _sibling_whole_copies))
            carried = launch(k, theirs[0, :SUBLANES, :LANES], carry)
            second_half[k] = functools.partial(g, chip_core, carried)
            return carried
        held.append((k, g, _sibling_copies))
        if k in ("w_down", "w_out", "w_attn_out"):
            return carry
        return launch(k, g[0, :SUBLANES, :LANES], carry)

    def assemble_start(keys, after, tag, carry=None):
        halves = []
        for k in keys:
            s32, st = pending[k]
            recv2 = _split_wait(f"chip_exchange_wait_{k}", st, _exchange_copies, after)[1]
            halves.append(_chip_sum_call(s32, recv2, chip_core, f"chip_sum_{k}"))
        (st,) = _split_start(f"grad_assemble_start_{tag}", [([], halves, len(halves), _assemble_copies)], core_arr, carry,
                             sibling_only=True)
        return st

    def assemble_wait(keys, st, after, tag):
        return dict(zip(keys, _split_wait(f"grad_assemble_wait_{tag}", st, _assemble_copies, after)))

    P = dict(b_gate=b_gate, rel_bias=rel_bias, conv_dw_b=conv_dw_b, conv_ln_g=conv_ln_g, conv_ln_b=conv_ln_b,
             norm_mix_pre=norm_mix_pre + launched, norm_mix_post=norm_mix_post, norm_ffn_pre=norm_ffn_pre,
             norm_ffn_post=norm_ffn_post, ffn_conv_b=ffn_conv_b)
    loss_tile, grad_x, G = _local_step(xs, target, P, late_weights, on_grad)

    small = [k for k in names if k not in big]
    packed = _pack([loss_tile[:1]] + [G[k] for k in small])
    (allsum,) = _split_start("small_grad_allsum_start",
                             [([packed], [jnp.zeros((N_DEVICES,) + packed.shape, F32)], N_DEVICES - 1, _allsum_copies)], core_arr)

    reduced, grads, deltas, new_m, new_v = {}, {}, {}, {}, {}

    def update(keys):
        for k in keys:
            gk, d, mn, vn = _adamw_call(weights[k][0], reduced[k], m_in[k][0], v_in[k][0], f"adamw_{k}")
            grads[k], deltas[k], new_m[k], new_v[k] = gk[None], d[None], mn[None], vn[None]

    reduced.update(assemble_wait(others, assembling["others"], [allsum["tile"], grad_x], "others"))
    update(others)
    assembling["w_in"] = assemble_start(["w_in"], [deltas[k] for k in others], "w_in")

    me_chip = jnp.stack([4 * xi + 2 * yi + ci, chip]).astype(jnp.int32)
    mine, landed = _split_wait("small_grad_allsum_wait", allsum, _allsum_copies, assembling["w_in"]["tile"])
    piece_shapes = [(1, LANES)] + [(G[k].size // LANES, LANES) if k == "rel_bias" else G[k].shape for k in small]
    piece_cols = [0] + [weights[k].shape[2] if k in ("conv_dw_w", "ffn_conv_w") else 0 for k in small]
    loss_row, *summed = _ordered_sum_call(mine, landed, me_chip, piece_shapes, piece_cols)
    loss = loss_row[0, 0]
    for k, gsum in zip(small, summed):
        grads[k] = gsum.reshape(weights[k].shape)
    ds, mns, vns = _adamw_small_call([weights[k] for k in small], [grads[k] for k in small],
                                     [m_in[k] for k in small], [v_in[k] for k in small])
    deltas.update(zip(small, ds))
    new_m.update(zip(small, mns))
    new_v.update(zip(small, vns))
    reduced.update(assemble_wait(["w_in"], assembling["w_in"], list(ds), "w_in"))
    update(["w_in"])

    return (loss, grad_x[None], *[grads[k] for k in names], *[deltas[k] for k in names],
            *[new_m[k] for k in names], *[new_v[k] for k in names])
```

```python
import functools
import math

import jax
import jax.numpy as jnp
import numpy as np
from jax import lax
from jax.experimental import pallas as pl
from jax.experimental.pallas import tpu as pltpu

F32 = jnp.float32
BF16 = jnp.bfloat16
MESH = pl.DeviceIdType.MESH

HEAD_DIM = 128
HEADS_PER_GROUP = 4
DILATED_PATTERNS = ((128, 1), (512, 4), (2048, 16))
N_GROUPS = 3
N_HEADS = N_GROUPS * HEADS_PER_GROUP
SPAN = 128
GROUP_WIDTH = HEADS_PER_GROUP * HEAD_DIM
CONV_WIDTH = 31
FFN_CONV_WIDTH = 3
N_BUCKETS = 32
MAX_DISTANCE = 2048
RMS_EPS = 1e-6
LN_EPS = 1e-5
NEG_INF = -1e30
ADAM_LR = 0.001
ADAM_B1 = 0.9
ADAM_B2 = 0.999
ADAM_EPS = 1e-08
ADAM_WD = 0.01
ADAM_STEP = 10

LANES = 128
SUBLANES = 8
PACKED_ROWS = 16
ROW_TILE = 512
GATE_ROWS, GATE_COLS = 512, 512
TIME_BLOCK = 128
CONV_PAD = 32
FFN_PAD = 8
VMEM_LIMIT = 56 << 20


def _params(sem=None, vmem=None):
    kw = {}
    if sem is not None:
        kw["dimension_semantics"] = sem
    if vmem is not None:
        kw["vmem_limit_bytes"] = vmem
    return pltpu.CompilerParams(**kw)


def _pick(n, cands):
    for c in cands:
        if n % c == 0:
            return c
    return n


ELEMENTWISE_TILE_BYTES = 3 << 19


def _row_tile(rows, cols):
    for align in (16, SUBLANES):
        fits = [t for t in range(align, rows + 1, align) if rows % t == 0 and t * cols * 4 <= ELEMENTWISE_TILE_BYTES]
        if fits:
            return max(fits)
    return SUBLANES


N_CHIPS = 4
M_TILES = (1024, 1408, 512, 256, 128)
N_TILES = (1024, 512, 1408, 256, 128)
K_TILES = (2176, 2048, 1408, 1024, 512, 256, 128)


def _matmul(a, b, mode, name, out_shards=False, tm=None):
    assert a.dtype == BF16 and b.dtype == BF16, (name, a.dtype, b.dtype)
    b3 = b.ndim == 3
    tn = tk = None
    halves = None
    if mode == "nn":
        M, K = a.shape
        N = b.shape[-1] * (N_CHIPS if b3 else 1)
        tn = b.shape[-1] if b3 else None
    elif mode == "nt":
        if a.ndim == 3:
            halves = a.shape[2]
        M, K = a.shape[-2], a.shape[-1] * (a.shape[0] if a.ndim == 3 else 1)
        N = b.shape[-2]
        tk = b.shape[-1] if b3 else None
    else:
        if b3:
            halves = b.shape[2]
        K, M = a.shape
        N = b.shape[-1] * (b.shape[0] if b3 else 1)
        tn = N // N_CHIPS if out_shards else None
    tm = tm or _pick(M, M_TILES)
    tn = tn or _pick(N, N_TILES)
    tk = tk or _pick(K, K_TILES)
    nk = K // tk
    dn = {"nn": (((1,), (0,)), ((), ())), "nt": (((1,), (1,)), ((), ())), "tn": (((0,), (0,)), ((), ()))}[mode]

    def body(a_ref, b_ref, o_ref):
        if nk == 1:
            o_ref[...] = lax.dot_general(a_ref[...], b_ref[...], dn, preferred_element_type=F32)
        else:
            @pl.when(pl.program_id(2) == 0)
            def _():
                o_ref[...] = jnp.zeros_like(o_ref)

            o_ref[...] += lax.dot_general(a_ref[...], b_ref[...], dn, preferred_element_type=F32)

    if mode == "tn":
        a_spec = pl.BlockSpec((tk, tm), lambda i, j, k: (k, i))
    elif halves:
        per = halves // tk
        a_spec = pl.BlockSpec((None, tm, tk), lambda i, j, k: (k // per, i, k % per))
    else:
        a_spec = pl.BlockSpec((tm, tk), lambda i, j, k: (i, k))
    if mode == "nn":
        b_spec = pl.BlockSpec((None, tk, tn), lambda i, j, k: (j, k, 0)) if b3 else pl.BlockSpec((tk, tn), lambda i, j, k: (k, j))
    elif mode == "nt":
        b_spec = pl.BlockSpec((None, tn, tk), lambda i, j, k: (k, j, 0)) if b3 else pl.BlockSpec((tn, tk), lambda i, j, k: (j, k))
    elif halves:
        per = halves // tn
        b_spec = pl.BlockSpec((None, tk, tn), lambda i, j, k: (j // per, k, j % per))
    else:
        b_spec = pl.BlockSpec((tk, tn), lambda i, j, k: (k, j))
    if out_shards:
        out_spec = pl.BlockSpec((None, tm, tn), lambda i, j, k: (j, i, 0))
        out_shape = jax.ShapeDtypeStruct((N_CHIPS, M, tn), F32)
    else:
        out_spec = pl.BlockSpec((tm, tn), lambda i, j, k: (i, j))
        out_shape = jax.ShapeDtypeStruct((M, N), F32)
    return pl.pallas_call(
        body, name=name, grid=(M // tm, N // tn, nk),
        in_specs=[a_spec, b_spec], out_specs=out_spec, out_shape=out_shape,
        compiler_params=_params(("parallel", "parallel", "arbitrary"), VMEM_LIMIT),
    )(a, b)


RING_SLOTS = 3


def _matmul_ring(a, b3, name):
    assert a.dtype == BF16 and b3.dtype == BF16, (name, a.dtype, b3.dtype)
    M, K = a.shape
    shards, _, tn = b3.shape
    tm = _pick(M, M_TILES)
    steps = (M // tm) * shards
    assert steps >= RING_SLOTS - 1, (name, steps)

    def body(a_ref, b_hbm, o_ref, ring_ref, sems):
        step = pl.program_id(0) * shards + pl.program_id(1)
        fetch = lambda t: pltpu.make_async_copy(b_hbm.at[t % shards], ring_ref.at[t % RING_SLOTS], sems.at[t % RING_SLOTS])

        @pl.when(step == 0)
        def _():
            for t in range(RING_SLOTS - 1):
                fetch(t).start()

        @pl.when(step + (RING_SLOTS - 1) < steps)
        def _():
            fetch(step + (RING_SLOTS - 1)).start()

        fetch(step).wait()
        o_ref[...] = lax.dot_general(a_ref[...], ring_ref[step % RING_SLOTS], (((1,), (0,)), ((), ())),
                                     preferred_element_type=F32)

    return pl.pallas_call(
        body, name=name, grid=(M // tm, shards),
        in_specs=[pl.BlockSpec((tm, K), lambda i, j: (i, 0)), ANY],
        out_specs=pl.BlockSpec((tm, tn), lambda i, j: (i, j)),
        out_shape=jax.ShapeDtypeStruct((M, shards * tn), F32),
        scratch_shapes=[pltpu.VMEM((RING_SLOTS, K, tn), BF16), pltpu.SemaphoreType.DMA((RING_SLOTS,))],
        compiler_params=_params(("arbitrary", "arbitrary"), VMEM_LIMIT),
    )(a, b3)


def _grad_half_matmul(a, name, chip_half, b, init=None):
    K, M = a.shape
    parts = b.ndim == 3
    N = b.shape[-1] * (b.shape[0] if parts else 1)
    h, tn = M // 2, N // N_CHIPS
    summed = init is not None
    dn = (((0,), (0,)), ((), ()))

    def body(ch_ref, a_ref, b_ref, *rest):
        product = lax.dot_general(a_ref[...], b_ref[...], dn, preferred_element_type=F32)
        if not summed:
            rest[0][...] = product
            return
        init_ref, own_ref, sum16_ref = rest
        total = product + init_ref[...]
        sum16_ref[...] = total.astype(BF16)

        @pl.when(pl.program_id(0) == ch_ref[0])
        def _():
            own_ref[...] = total

    if parts:
        per = b.shape[2] // tn
        b_spec = pl.BlockSpec((None, K, tn), lambda j, ch_ref: (j // per, 0, j % per))
    else:
        b_spec = pl.BlockSpec((K, tn), lambda j, ch_ref: (0, j))
    shard_spec = pl.BlockSpec((None, h, tn), lambda j, ch_ref: (j, 0, 0))
    own_spec = pl.BlockSpec((h, tn), lambda j, ch_ref: (0, 0))
    shape = (N_CHIPS, h, tn)
    return pl.pallas_call(
        body, name=name + ("_mine" if summed else "_theirs"),
        grid_spec=pltpu.PrefetchScalarGridSpec(
            num_scalar_prefetch=1, grid=(N_CHIPS,),
            in_specs=[pl.BlockSpec((K, h), lambda j, ch_ref: (0, ch_ref[1])), b_spec] + [shard_spec] * summed,
            out_specs=[own_spec, shard_spec] if summed else shard_spec),
        out_shape=[jax.ShapeDtypeStruct((h, tn), F32), jax.ShapeDtypeStruct(shape, BF16)] if summed
        else jax.ShapeDtypeStruct(shape, F32),
        compiler_params=_params(("arbitrary",), VMEM_LIMIT),
    )(chip_half, a, b, *([init] if summed else []))


def _rms(x, g):
    r = lax.rsqrt(jnp.mean(x * x, axis=-1, keepdims=True) + RMS_EPS)
    return x * r * g


def _rms_bwd(x, g, dy):
    r = lax.rsqrt(jnp.mean(x * x, axis=-1, keepdims=True) + RMS_EPS)
    n = x * r
    dn = dy * g
    dx = r * (dn - n * jnp.mean(dn * n, axis=-1, keepdims=True))
    return dx, jnp.sum(dy * n, axis=0, keepdims=True)


def _sigmoid(x):
    return 1.0 / (1.0 + jnp.exp(-x))


_GELU_C = math.sqrt(2.0 / math.pi)


def _gelu(x):
    return 0.5 * x * (1.0 + jnp.tanh(_GELU_C * (x + 0.044715 * x * x * x)))


def _gelu_and_grad(x):
    x2 = x * x
    t = jnp.tanh(_GELU_C * x * (1.0 + 0.044715 * x2))
    half = 0.5 * (1.0 + t)
    return x * half, half + (0.5 * _GELU_C) * x * (1.0 - t * t) * (1.0 + (3.0 * 0.044715) * x2)


def _row_spec(width, col_block=0):
    return pl.BlockSpec((ROW_TILE, width), lambda i: (i, col_block))


def _vec_spec(width, col_block=0):
    return pl.BlockSpec((1, width), lambda i: (0, col_block))


def _accumulate(ref, part):
    @pl.when(pl.program_id(0) == 0)
    def _():
        ref[...] = part

    @pl.when(pl.program_id(0) > 0)
    def _():
        ref[...] += part


def _rms_fwd_call(x, g):
    S, D = x.shape

    def body(x_ref, g_ref, h_ref):
        h_ref[...] = _rms(x_ref[...], g_ref[...]).astype(BF16)

    return pl.pallas_call(
        body, name="rms_mix_pre", grid=(S // ROW_TILE,),
        in_specs=[_row_spec(D), _vec_spec(D)], out_specs=_row_spec(D),
        out_shape=jax.ShapeDtypeStruct((S, D), BF16),
        compiler_params=_params(("parallel",)),
    )(x, g)


def _ln_silu_call(c1, g, b):
    S, C = c1.shape

    def body(c_ref, g_ref, b_ref, o_ref):
        xv = c_ref[...]
        mu = jnp.mean(xv, axis=-1, keepdims=True)
        xc = xv - mu
        var = jnp.mean(xc * xc, axis=-1, keepdims=True)
        z = xc * lax.rsqrt(var + LN_EPS) * g_ref[...] + b_ref[...]
        o_ref[...] = (z * _sigmoid(z)).astype(BF16)

    return pl.pallas_call(
        body, name="conv_ln_silu", grid=(S // ROW_TILE,),
        in_specs=[_row_spec(C), _vec_spec(C), _vec_spec(C)], out_specs=_row_spec(C),
        out_shape=jax.ShapeDtypeStruct((S, C), BF16),
        compiler_params=_params(("parallel",)),
    )(c1, g, b)


def _ln_silu_bwd_call(c1, g, b, dc):
    S, C = c1.shape

    def body(c_ref, g_ref, b_ref, dc_ref, dx_ref, dg_ref, db_ref):
        xv = c_ref[...]
        mu = jnp.mean(xv, axis=-1, keepdims=True)
        xc = xv - mu
        rs = lax.rsqrt(jnp.mean(xc * xc, axis=-1, keepdims=True) + LN_EPS)
        xh = xc * rs
        z = xh * g_ref[...] + b_ref[...]
        sg = _sigmoid(z)
        dz = dc_ref[...] * (sg * (1.0 + z * (1.0 - sg)))
        dxh = dz * g_ref[...]
        dx_ref[...] = rs * (dxh - jnp.mean(dxh, axis=-1, keepdims=True) - xh * jnp.mean(dxh * xh, axis=-1, keepdims=True))
        _accumulate(dg_ref, jnp.sum(dz * xh, axis=0, keepdims=True))
        _accumulate(db_ref, jnp.sum(dz, axis=0, keepdims=True))

    return pl.pallas_call(
        body, name="conv_ln_silu_bwd", grid=(S // ROW_TILE,),
        in_specs=[_row_spec(C), _vec_spec(C), _vec_spec(C), _row_spec(C)],
        out_specs=[_row_spec(C), _vec_spec(C), _vec_spec(C)],
        out_shape=[jax.ShapeDtypeStruct((S, C), F32), jax.ShapeDtypeStruct((1, C), F32), jax.ShapeDtypeStruct((1, C), F32)],
        compiler_params=_params(("arbitrary",)),
    )(c1, g, b, dc)


def _mix_call(proj, gate_col0, b_gate, y_a, y_c):
    S, D = y_a.shape
    w = GATE_COLS
    nc = D // w
    ga0, gc0 = gate_col0 // w, (gate_col0 + D) // w

    def body(ga_ref, gc_ref, ba_ref, bc_ref, ya_ref, yc_ref, o_ref):
        o_ref[...] = (_sigmoid(ga_ref[...] + ba_ref[...]) * ya_ref[...]
                      + _sigmoid(gc_ref[...] + bc_ref[...]) * yc_ref[...]).astype(BF16)

    tile = lambda off: pl.BlockSpec((GATE_ROWS, w), lambda i, j: (i, off + j))
    vec = lambda off: pl.BlockSpec((1, w), lambda i, j: (0, off + j))
    return pl.pallas_call(
        body, name="gate_mix", grid=(S // GATE_ROWS, nc),
        in_specs=[tile(ga0), tile(gc0), vec(0), vec(nc), tile(0), tile(0)],
        out_specs=tile(0), out_shape=jax.ShapeDtypeStruct((S, D), BF16),
        compiler_params=_params(("parallel", "parallel")),
    )(proj, proj, b_gate, b_gate, y_a, y_c)


def _window_stores(stage_ref, slot, dst_ref, rows, cols, sems):
    width = stage_ref.shape[-1]
    return [pltpu.make_async_copy(stage_ref.at[slot, p], dst_ref.at[rows, pl.ds(pl.multiple_of(c, LANES), width)], sems.at[slot, p])
            for p, c in enumerate(cols)]


def _staged_window_stores(stage_ref, dst_ref, sems, step, n_steps, rows, cols, fill):
    slot = step % 2
    copies = lambda s: _window_stores(stage_ref, s, dst_ref, rows, cols, sems)

    @pl.when(step >= 2)
    def _():
        for cp in copies(slot):
            cp.wait()

    fill(slot)
    for cp in copies(slot):
        cp.start()

    @pl.when(step == n_steps - 1)
    def _():
        for s in ([slot, 1 - slot] if n_steps > 1 else [slot]):
            for cp in copies(s):
                cp.wait()


def _mix_bwd_call(dmixed, proj, gate_col0, b_gate, y_a, y_c):
    S, D = y_a.shape
    w = GATE_COLS
    nc = D // w
    nr = S // GATE_ROWS
    ga0, gc0 = gate_col0 // w, (gate_col0 + D) // w

    def body(dm_ref, ga_ref, gc_ref, ba_ref, bc_ref, ya_ref, yc_ref, dya_ref, dyc_ref, dproj_ref, dba_ref, dbc_ref,
             stage_ref, sems):
        j, i = pl.program_id(0), pl.program_id(1)
        dm = dm_ref[...]
        sa = _sigmoid(ga_ref[...] + ba_ref[...])
        sc = _sigmoid(gc_ref[...] + bc_ref[...])
        dya_ref[...] = (dm * sa).astype(BF16)
        dyc_ref[...] = (dm * sc).astype(BF16)
        dga = dm * ya_ref[...] * sa * (1.0 - sa)
        dgc = dm * yc_ref[...] * sc * (1.0 - sc)

        def fill(slot):
            stage_ref[slot, 0] = dga.astype(BF16)
            stage_ref[slot, 1] = dgc.astype(BF16)

        rows = pl.ds(pl.multiple_of(i * GATE_ROWS, GATE_ROWS), GATE_ROWS)
        _staged_window_stores(stage_ref, dproj_ref, sems, j * nr + i, nc * nr, rows,
                              [gate_col0 + j * w, gate_col0 + D + j * w], fill)
        pa = jnp.sum(dga, axis=0, keepdims=True)
        pc = jnp.sum(dgc, axis=0, keepdims=True)

        @pl.when(i == 0)
        def _():
            dba_ref[...] = pa
            dbc_ref[...] = pc

        @pl.when(i > 0)
        def _():
            dba_ref[...] += pa
            dbc_ref[...] += pc

    tile = lambda off: pl.BlockSpec((GATE_ROWS, w), lambda j, i: (i, off + j))
    vec = lambda off: pl.BlockSpec((1, w), lambda j, i: (0, off + j))
    return pl.pallas_call(
        body, name="gate_mix_bwd", grid=(nc, nr),
        in_specs=[tile(0), tile(ga0), tile(gc0), vec(0), vec(nc), tile(0), tile(0)],
        out_specs=[tile(0), tile(0), ANY, vec(0), vec(0)],
        out_shape=[jax.ShapeDtypeStruct((S, D), BF16)] * 2 + [jax.ShapeDtypeStruct((S, proj.shape[1]), BF16)] + [
                   jax.ShapeDtypeStruct((1, D), F32), jax.ShapeDtypeStruct((1, D), F32)],
        scratch_shapes=[pltpu.VMEM((2, 2, GATE_ROWS, w), BF16), pltpu.SemaphoreType.DMA((2, 2))],
        compiler_params=_params(("arbitrary", "arbitrary")),
    )(dmixed, proj, proj, b_gate, b_gate, y_a, y_c)


def _res1_call(x, out, g_post, g_pre):
    S, D = x.shape

    def body(x_ref, o_ref, gp_ref, gq_ref, x1_ref, h2_ref):
        x1 = x_ref[...] + _rms(o_ref[...], gp_ref[...])
        x1_ref[...] = x1
        h2_ref[...] = _rms(x1, gq_ref[...]).astype(BF16)

    return pl.pallas_call(
        body, name="residual_mix", grid=(S // ROW_TILE,),
        in_specs=[_row_spec(D), _row_spec(D), _vec_spec(D), _vec_spec(D)],
        out_specs=[_row_spec(D), _row_spec(D)],
        out_shape=[jax.ShapeDtypeStruct((S, D), F32), jax.ShapeDtypeStruct((S, D), BF16)],
        compiler_params=_params(("parallel",)),
    )(x, out, g_post, g_pre)


def _loss_call(y, x1, g_post, target):
    S, D = y.shape

    def body(y_ref, x1_ref, g_ref, t_ref, loss_ref, dx_ref, dy_ref, dg_ref):
        yv, gv = y_ref[...], g_ref[...]
        err = x1_ref[...] + _rms(yv, gv) - t_ref[...]
        dx2 = err * (1.0 / D)
        dx_ref[...] = dx2
        dy, dg = _rms_bwd(yv, gv, dx2)
        dy_ref[...] = dy.astype(BF16)
        _accumulate(dg_ref, dg)
        part = 0.5 * jnp.sum(jnp.mean(err * err, axis=-1, keepdims=True), axis=0, keepdims=True)
        _accumulate(loss_ref, jnp.broadcast_to(part, (SUBLANES, LANES)))

    return pl.pallas_call(
        body, name="residual_ffn_loss", grid=(S // ROW_TILE,),
        in_specs=[_row_spec(D), _row_spec(D), _vec_spec(D), _row_spec(D)],
        out_specs=[pl.BlockSpec((SUBLANES, LANES), lambda i: (0, 0)), _row_spec(D), _row_spec(D), _vec_spec(D)],
        out_shape=[jax.ShapeDtypeStruct((SUBLANES, LANES), F32), jax.ShapeDtypeStruct((S, D), F32),
                   jax.ShapeDtypeStruct((S, D), BF16), jax.ShapeDtypeStruct((1, D), F32)],
        compiler_params=_params(("arbitrary",)),
    )(y, x1, g_post, target)


def _mid_bwd_call(x1, g_pre, dh2, dx2, out, g_post):
    S, D = x1.shape

    def body(x1_ref, gq_ref, dh_ref, dx2_ref, o_ref, gp_ref, dx1_ref, do_ref, dgq_ref, dgp_ref):
        d, dgq = _rms_bwd(x1_ref[...], gq_ref[...], dh_ref[...])
        dx1 = dx2_ref[...] + d
        dx1_ref[...] = dx1
        do, dgp = _rms_bwd(o_ref[...], gp_ref[...], dx1)
        do_ref[...] = do.astype(BF16)
        _accumulate(dgq_ref, dgq)
        _accumulate(dgp_ref, dgp)

    return pl.pallas_call(
        body, name="residual_mix_bwd", grid=(S // ROW_TILE,),
        in_specs=[_row_spec(D), _vec_spec(D), _row_spec(D), _row_spec(D), _row_spec(D), _vec_spec(D)],
        out_specs=[_row_spec(D), _row_spec(D), _vec_spec(D), _vec_spec(D)],
        out_shape=[jax.ShapeDtypeStruct((S, D), F32), jax.ShapeDtypeStruct((S, D), BF16)] + [jax.ShapeDtypeStruct((1, D), F32)] * 2,
        compiler_params=_params(("arbitrary",)),
    )(x1, g_pre, dh2, dx2, out, g_post)


def _in_bwd_call(x, g, dh1, dx1):
    S, D = x.shape

    def body(x_ref, g_ref, dh_ref, dx1_ref, gx_ref, dg_ref):
        d, dg = _rms_bwd(x_ref[...], g_ref[...], dh_ref[...])
        gx_ref[...] = dx1_ref[...] + d
        _accumulate(dg_ref, dg)

    return pl.pallas_call(
        body, name="rms_mix_pre_bwd", grid=(S // ROW_TILE,),
        in_specs=[_row_spec(D), _vec_spec(D), _row_spec(D), _row_spec(D)],
        out_specs=[_row_spec(D), _vec_spec(D)],
        out_shape=[jax.ShapeDtypeStruct((S, D), F32), jax.ShapeDtypeStruct((1, D), F32)],
        compiler_params=_params(("arbitrary",)),
    )(x, g, dh1, dx1)


def _bucket_table(dilation):
    qi = np.arange(SPAN)[:, None]
    ki = np.arange(2 * SPAN)[None, :]
    dist = np.maximum(qi + SPAN - ki, 0) * dilation
    max_exact = N_BUCKETS // 2
    d = np.maximum(dist, 1).astype(np.float64)
    large = max_exact + (np.log(d / max_exact) / math.log(MAX_DISTANCE / max_exact) * (N_BUCKETS - max_exact)).astype(np.int32)
    large = np.minimum(large, N_BUCKETS - 1)
    return np.where(dist < max_exact, dist, large).astype(np.int32)


def _bucket_tables():
    return jnp.asarray(np.stack([_bucket_table(r) for _, r in DILATED_PATTERNS]))


def _bias_table_call(rel_bias, buckets):
    def body(rb_ref, bk_ref, o_ref):
        for h in range(N_HEADS):
            bk = bk_ref[h // HEADS_PER_GROUP]

            def step(b, acc):
                return jnp.where(bk == b, rb_ref[b, h], acc)

            o_ref[h] = lax.fori_loop(0, N_BUCKETS, step, jnp.zeros((SPAN, 2 * SPAN), F32))

    return pl.pallas_call(
        body, name="rel_bias_table",
        in_specs=[pl.BlockSpec(memory_space=pltpu.SMEM), pl.BlockSpec(memory_space=pltpu.VMEM)],
        out_specs=pl.BlockSpec(memory_space=pltpu.VMEM),
        out_shape=jax.ShapeDtypeStruct((N_HEADS, SPAN, 2 * SPAN), F32),
    )(rel_bias, buckets)


def _bias_grad_call(dbias, buckets):
    def body(db_ref, bk_ref, o_ref, rows_ref):
        for h in range(N_HEADS):
            bk = bk_ref[h // HEADS_PER_GROUP]
            dv = db_ref[h]

            def step(b, carry):
                rows_ref[h, b] = jnp.sum(jnp.where(bk == b, dv, 0.0), axis=0, keepdims=True)
                return carry

            lax.fori_loop(0, N_BUCKETS, step, 0)
        o_ref[...] = jnp.sum(rows_ref[...], axis=-1, keepdims=True)

    out = pl.pallas_call(
        body, name="rel_bias_grad",
        in_specs=[pl.BlockSpec(memory_space=pltpu.VMEM), pl.BlockSpec(memory_space=pltpu.VMEM)],
        out_specs=pl.BlockSpec(memory_space=pltpu.VMEM),
        out_shape=jax.ShapeDtypeStruct((N_HEADS, N_BUCKETS, 1, 1), F32),
        scratch_shapes=[pltpu.VMEM((N_HEADS, N_BUCKETS, 1, 2 * SPAN), F32)],
    )(dbias, buckets)
    return out.reshape(N_HEADS, N_BUCKETS).T


def _dot_nt(a, b):
    return lax.dot_general(a, b, (((1,), (1,)), ((), ())), preferred_element_type=F32)


def _dot_nn(a, b):
    return lax.dot_general(a, b, (((1,), (0,)), ((), ())), preferred_element_type=F32)


def _dot_tn(a, b):
    return lax.dot_general(a, b, (((0,), (0,)), ((), ())), preferred_element_type=F32)


def _band_masks(n, nb):
    qi = lax.broadcasted_iota(jnp.int32, (SPAN, SPAN), 0)
    ki = lax.broadcasted_iota(jnp.int32, (SPAN, SPAN), 1)
    prev_ok = jnp.logical_and(ki >= qi, n > 0)
    cur_ok = ki <= qi
    next_ok = jnp.logical_and(ki >= qi, n < nb - 1)
    return prev_ok, cur_ok, next_ok


def _wide_band_mask(n):
    qi = lax.broadcasted_iota(jnp.int32, (SPAN, 2 * SPAN), 0)
    ki = lax.broadcasted_iota(jnp.int32, (SPAN, 2 * SPAN), 1)
    prev_ok = jnp.logical_and(jnp.logical_and(ki < SPAN, ki >= qi), n > 0)
    cur_ok = jnp.logical_and(ki >= SPAN, ki - SPAN <= qi)
    return jnp.logical_or(prev_ok, cur_ok)


def _attn_plan(S, group):
    r = DILATED_PATTERNS[group][1]
    hp, per = (HEADS_PER_GROUP, 1) if r == 1 else (2, 4)
    return r, S // (r * SPAN), hp, per


def _residue_rows(rho, r):
    return slice(None) if r == 1 else pl.ds(rho, SPAN, stride=r)


def _for_residues(r, per, fn):
    if r == per:
        for u in range(per):
            fn(u)
        return

    def step(i, carry):
        for u in range(per):
            fn(i * per + u)
        return carry

    lax.fori_loop(0, r // per, step, 0)


def _attn_fwd_call(proj, bias, group, merge_with=None):
    S = proj.shape[0]
    r, nb, hp, per = _attn_plan(S, group)
    scale = HEAD_DIM ** -0.5
    kinds = ("q", "kp", "kc", "vp", "vc") if nb > 1 else ("q", "kc", "vc")
    merge = merge_with is not None
    assert not merge or (r == 1 and hp == HEADS_PER_GROUP and len(merge_with) == 4)

    per_kind = _refs_per_kind(r, hp)

    def body(*refs):
        ins = {kind: refs[i * per_kind:(i + 1) * per_kind] for i, kind in enumerate(kinds)}
        b_ref, *rest = refs[len(kinds) * per_kind:]
        if merge:
            o2_ref, s2_ref, o3_ref, s3_ref, a_ref, ab_ref, lse_ref = rest
        else:
            o_ref, lse_ref = rest
        n = pl.program_id(1)
        prev_ok, cur_ok, _ = _band_masks(n, nb)

        band_ok = _wide_band_mask(n) if nb > 1 else cur_ok

        def residue(rho):
            rows = _residue_rows(rho, r)
            for j in range(hp):
                get = lambda kind: _head_rows(ins[kind], j, rows, r).astype(BF16)
                q = get("q")
                if nb > 1:
                    keys, vals, bias_j = jnp.concatenate([get("kp"), get("kc")], axis=0), jnp.concatenate([get("vp"), get("vc")], axis=0), b_ref[j]
                else:
                    keys, vals, bias_j = get("kc"), get("vc"), b_ref[j, :, SPAN:]
                s = jnp.where(band_ok, _dot_nt(q, keys) * scale + bias_j, NEG_INF)
                m = jnp.max(s, axis=-1, keepdims=True)
                p = jnp.exp(s - m)
                den = jnp.sum(p, axis=-1, keepdims=True)
                o1 = _dot_nn(p.astype(BF16), vals) / den
                s1 = jnp.broadcast_to(m + jnp.log(den), (SPAN, HEAD_DIM))
                if not merge:
                    o_ref[j, rows, :] = o1
                    lse_ref[j, rows, :] = s1
                    continue
                sl = slice(j * HEAD_DIM, (j + 1) * HEAD_DIM)
                s2, s3 = s2_ref[j], s3_ref[j]
                mx = jnp.maximum(jnp.maximum(s1, s2), s3)
                w1 = jnp.exp(s1 - mx)
                w2 = jnp.exp(s2 - mx)
                w3 = jnp.exp(s3 - mx)
                total = w1 + w2 + w3
                merged = (w1 * o1 + w2 * o2_ref[j] + w3 * o3_ref[j]) / total
                a_ref[:, sl] = merged
                ab_ref[:, sl] = merged.astype(BF16)
                lse_ref[:, sl] = mx + jnp.log(total)

        _for_residues(r, per, residue)

    in_specs = [_head_spec(r, nb, hp, kind, group, jj) for kind in kinds for jj in range(per_kind)]
    in_specs.append(pl.BlockSpec((hp, SPAN, 2 * SPAN), lambda j, n: (group * (HEADS_PER_GROUP // hp) + j, 0, 0)))
    out = pl.BlockSpec((hp, r * SPAN, HEAD_DIM), lambda j, n: (j, n, 0))
    operands = [proj] * (len(in_specs) - 1) + [bias]
    if merge:
        in_specs += [out] * 4
        operands += list(merge_with)
        rows = pl.BlockSpec((SPAN, GROUP_WIDTH), lambda j, n: (n, 0))
        out_specs = [rows] * 3
        out_shape = [jax.ShapeDtypeStruct((S, GROUP_WIDTH), dt) for dt in (F32, BF16, F32)]
    else:
        out_specs = [out] * 2
        out_shape = [jax.ShapeDtypeStruct((HEADS_PER_GROUP, S, HEAD_DIM), F32)] * 2
    return pl.pallas_call(
        body, name=f"attn_fwd_g{group}", grid=(HEADS_PER_GROUP // hp, nb),
        in_specs=in_specs, out_specs=out_specs, out_shape=out_shape,
        compiler_params=_params(("parallel", "parallel"), VMEM_LIMIT),
    )(*operands)


_PROJ_PART = dict(q=0, qn=0, kp=1, kc=1, vp=2, vc=2)


def _refs_per_kind(r, hp):
    return 1 if r == 1 else hp


def _head_rows(refs, j, rows, r):
    return refs[0][:, j * HEAD_DIM:(j + 1) * HEAD_DIM] if r == 1 else refs[j][rows, :]


def _head_spec(r, nb, hp, kind, group, jj):
    if kind in _PROJ_PART:
        base = (_PROJ_PART[kind] * N_GROUPS + group) * HEADS_PER_GROUP
    else:
        base = 0
    if kind.endswith("p"):
        row = lambda n: jnp.maximum(n - 1, 0)
    elif kind.endswith("n"):
        row = lambda n: jnp.minimum(n + 1, nb - 1)
    else:
        row = lambda n: n
    if r == 1:
        return pl.BlockSpec((SPAN, hp * HEAD_DIM), lambda j, n: (row(n), base // hp + j))
    return pl.BlockSpec((r * SPAN, HEAD_DIM), lambda j, n: (row(n), base + j * hp + jj))


def _attn_bwd_call(proj, bias, a, da, lse, group, dproj):
    S = proj.shape[0]
    r, nb, hp, per = _attn_plan(S, group)
    scale = HEAD_DIM ** -0.5
    kinds = ("q", "qn", "kp", "kc", "vp", "vc", "da", "dan", "lse", "lsen", "a", "an") if nb > 1 else ("q", "kc", "vc", "da", "lse", "a")
    source = dict(da=da, dan=da, lse=lse, lsen=lse, a=a, an=a)

    per_kind = _refs_per_kind(r, hp)
    per_group = HEADS_PER_GROUP // hp
    block_rows = r * SPAN

    def body(*refs):
        ins = {kind: refs[i * per_kind:(i + 1) * per_kind] for i, kind in enumerate(kinds)}
        b_ref, _, dproj_ref, db_ref, stage_ref, sems = refs[len(kinds) * per_kind:][:6]
        strided_ref = None if r == 1 else refs[-1]
        jg, n = pl.program_id(0), pl.program_id(1)
        prev_ok, cur_ok, next_ok = _band_masks(n, nb)

        @pl.when(n == 0)
        def _():
            db_ref[...] = jnp.zeros_like(db_ref)

        band_ok = _wide_band_mask(n) if nb > 1 else cur_ok

        def fill(slot):
            def put(part, j, rows, value):
                if r == 1:
                    stage_ref[slot, part, :, j * HEAD_DIM:(j + 1) * HEAD_DIM] = value.astype(BF16)
                else:
                    strided_ref[part, j, rows, :] = value

            _for_residues(r, per, functools.partial(residue, put))
            if r > 1:
                for part in range(3):
                    for j in range(hp):
                        for t0 in range(0, block_rows, ROW_TILE):
                            stage_ref[slot, part, t0:t0 + ROW_TILE, j * HEAD_DIM:(j + 1) * HEAD_DIM] = (
                                strided_ref[part, j, t0:t0 + ROW_TILE, :].astype(BF16))

        def residue(put, rho):
            rows = _residue_rows(rho, r)
            for j in range(hp):
                get = lambda kind: _head_rows(ins[kind], j, rows, r)
                q = get("q").astype(BF16)
                kc = get("kc").astype(BF16)
                vc = get("vc").astype(BF16)
                da_q = get("da")
                dav = da_q.astype(BF16)
                lse_q = get("lse")
                dl_q = jnp.sum(get("a") * da_q, axis=-1, keepdims=True)
                if nb == 1:
                    pc = jnp.exp(jnp.where(cur_ok, _dot_nt(q, kc) * scale + b_ref[j, :, SPAN:], NEG_INF) - lse_q)
                    dsc = pc * (_dot_nt(dav, vc) - dl_q)
                    dsc_b = dsc.astype(BF16)
                    dq = _dot_nn(dsc_b, kc)
                    dk = _dot_tn(dsc_b, q)
                    dv = _dot_tn(pc.astype(BF16), dav)
                    db_ref[j, :, SPAN:] += dsc
                else:
                    qn = get("qn").astype(BF16)
                    da_n = get("dan")
                    dan = da_n.astype(BF16)
                    keys = jnp.concatenate([get("kp").astype(BF16), kc], axis=0)
                    vals = jnp.concatenate([get("vp").astype(BF16), vc], axis=0)
                    wide = lambda t: jnp.concatenate([t, t], axis=1)
                    p = jnp.exp(jnp.where(band_ok, _dot_nt(q, keys) * scale + b_ref[j], NEG_INF) - wide(lse_q))
                    ds = p * (_dot_nt(dav, vals) - dl_q)
                    dq = _dot_nn(ds.astype(BF16), keys)
                    db_ref[j] += ds
                    pn = jnp.exp(jnp.where(next_ok, _dot_nt(qn, kc) * scale + b_ref[j, :, :SPAN], NEG_INF) - get("lsen"))
                    dsn = pn * (_dot_nt(dan, vc) - jnp.sum(get("an") * da_n, axis=-1, keepdims=True))
                    both = lambda cur_part, next_part: jnp.concatenate([cur_part.astype(BF16), next_part.astype(BF16)], axis=0)
                    dk = _dot_tn(both(ds[:, SPAN:], dsn), jnp.concatenate([q, qn], axis=0))
                    dv = _dot_tn(both(p[:, SPAN:], pn), jnp.concatenate([dav, dan], axis=0))
                put(0, j, rows, dq * scale)
                put(1, j, rows, dk * scale)
                put(2, j, rows, dv)

        cols = [(part * N_GROUPS + group) * GROUP_WIDTH + jg * (hp * HEAD_DIM) for part in range(3)]
        rows = pl.ds(pl.multiple_of(n * block_rows, SPAN), block_rows)
        _staged_window_stores(stage_ref, dproj_ref, sems, jg * nb + n, per_group * nb, rows, cols, fill)

    band = (hp, SPAN, 2 * SPAN)
    in_specs = [_head_spec(r, nb, hp, kind, group, jj) for kind in kinds for jj in range(per_kind)]
    in_specs += [pl.BlockSpec(band, lambda j, n: (group * per_group + j, 0, 0)), ANY]
    operands = [source.get(kind, proj) for kind in kinds for _ in range(per_kind)] + [bias, dproj]
    scratch = [pltpu.VMEM((2, 3, block_rows, hp * HEAD_DIM), BF16), pltpu.SemaphoreType.DMA((2, 3))]
    if r > 1:
        scratch.append(pltpu.VMEM((3, hp, block_rows, HEAD_DIM), F32))
    return pl.pallas_call(
        body, name=f"attn_bwd_g{group}", grid=(per_group, nb),
        in_specs=in_specs,
        out_specs=[ANY, pl.BlockSpec(band, lambda j, n: (j, 0, 0))],
        out_shape=[jax.ShapeDtypeStruct(dproj.shape, BF16), jax.ShapeDtypeStruct((HEADS_PER_GROUP, SPAN, 2 * SPAN), F32)],
        input_output_aliases={len(operands) - 1: 0},
        scratch_shapes=scratch,
        compiler_params=_params(("arbitrary", "arbitrary"), VMEM_LIMIT),
    )(*operands)


def _tap_rows(xpad_ref, t0, k, width, pad):
    return xpad_ref[pl.ds(t0 + (pad - (width - 1 - k)), TIME_BLOCK), :]


def _conv_block(xpad_ref, t0, w_ref, width, pad):
    acc = None
    for k in range(width):
        term = w_ref[k:k + 1, :] * _tap_rows(xpad_ref, t0, k, width, pad)
        acc = term if acc is None else acc + term
    return acc


def _conv_transpose_block(dpad_ref, t0, w_ref, width):
    acc = None
    for k in range(width):
        term = w_ref[k:k + 1, :] * dpad_ref[pl.ds(t0 + (width - 1 - k), TIME_BLOCK), :]
        acc = term if acc is None else acc + term
    return acc


def _conv_weight_grad(xpad_ref, t0, dy, dw_ref, width, pad):
    for k in range(width):
        dw_ref[k:k + 1, :] += jnp.sum(dy * _tap_rows(xpad_ref, t0, k, width, pad), axis=0, keepdims=True)


def _time_loop(S, step, skip_first=0, skip_last=0):
    def it(tb, carry):
        step(pl.multiple_of(tb * TIME_BLOCK, TIME_BLOCK))
        return carry

    lax.fori_loop(skip_first, S // TIME_BLOCK - skip_last, it, 0)


def _fill_head(head_ref, x_ref, pad):
    head_ref[0:pad, :] = jnp.zeros((pad, LANES), F32)
    head_ref[pad:, :] = x_ref[0:TIME_BLOCK, :]


def _fill_tail(tail_ref, x_ref, pad):
    S = x_ref.shape[0]
    tail_ref[0:TIME_BLOCK, :] = x_ref[S - TIME_BLOCK:S, :]
    tail_ref[TIME_BLOCK:, :] = jnp.zeros((pad, LANES), F32)


def _conv_fwd_call(proj, col0, w, b):
    S = proj.shape[0]
    C = w.shape[1]
    nt = C // LANES
    v0, g0 = col0 // LANES, (col0 + C) // LANES

    def body(val_ref, gate_ref, w_ref, b_ref, o_ref, pad_ref):
        pad_ref[0:CONV_PAD, :] = jnp.zeros((CONV_PAD, LANES), F32)
        pad_ref[CONV_PAD:, :] = val_ref[...] * _sigmoid(gate_ref[...])

        def step(t0):
            o_ref[pl.ds(t0, TIME_BLOCK), :] = _conv_block(pad_ref, t0, w_ref, CONV_WIDTH, CONV_PAD) + b_ref[...]

        _time_loop(S, step)

    seq = lambda off: pl.BlockSpec((S, LANES), lambda i: (0, off + i))
    return pl.pallas_call(
        body, name="conv_module", grid=(nt,),
        in_specs=[seq(v0), seq(g0), pl.BlockSpec((CONV_WIDTH, LANES), lambda i: (0, i)), pl.BlockSpec((1, LANES), lambda i: (0, i))],
        out_specs=seq(0), out_shape=jax.ShapeDtypeStruct((S, C), F32),
        scratch_shapes=[pltpu.VMEM((S + CONV_PAD, LANES), F32)],
        compiler_params=_params(("parallel",)),
    )(proj, proj, w, b)


def _conv_bwd_call(proj, col0, w, dc1, dproj):
    S = proj.shape[0]
    C = w.shape[1]
    nt = C // LANES
    v0, g0 = col0 // LANES, (col0 + C) // LANES

    def body(val_ref, gate_ref, w_ref, dy_ref, _, dproj_ref, dw_ref, db_ref, xpad_ref, tail_ref, dwacc_ref, stage_ref, sems):
        i = pl.program_id(0)
        xpad_ref[0:CONV_PAD, :] = jnp.zeros((CONV_PAD, LANES), F32)
        xpad_ref[CONV_PAD:, :] = val_ref[...] * _sigmoid(gate_ref[...])
        _fill_tail(tail_ref, dy_ref, CONV_PAD)
        dwacc_ref[...] = jnp.zeros_like(dwacc_ref)

        def fill(slot):
            def block(t0, dy_src, dy_t0):
                rows = pl.ds(t0, TIME_BLOCK)
                _conv_weight_grad(xpad_ref, t0, dy_ref[rows, :], dwacc_ref, CONV_WIDTH, CONV_PAD)
                dc0 = _conv_transpose_block(dy_src, dy_t0, w_ref, CONV_WIDTH)
                sg = _sigmoid(gate_ref[rows, :])
                stage_ref[slot, 0, rows, :] = (dc0 * sg).astype(BF16)
                stage_ref[slot, 1, rows, :] = (dc0 * val_ref[rows, :] * sg * (1.0 - sg)).astype(BF16)

            _time_loop(S, lambda t0: block(t0, dy_ref, t0), skip_last=1)
            block(S - TIME_BLOCK, tail_ref, 0)

        _staged_window_stores(stage_ref, dproj_ref, sems, i, nt, pl.ds(0, S),
                              [col0 + i * LANES, col0 + C + i * LANES], fill)
        dw_ref[...] = dwacc_ref[...]
        db_ref[...] = jnp.sum(dy_ref[...], axis=0, keepdims=True)

    seq = lambda off: pl.BlockSpec((S, LANES), lambda i: (0, off + i))
    return pl.pallas_call(
        body, name="conv_module_bwd", grid=(nt,),
        in_specs=[seq(v0), seq(g0), pl.BlockSpec((CONV_WIDTH, LANES), lambda i: (0, i)), seq(0), ANY],
        out_specs=[ANY, pl.BlockSpec((CONV_PAD, LANES), lambda i: (0, i)), pl.BlockSpec((1, LANES), lambda i: (0, i))],
        out_shape=[jax.ShapeDtypeStruct(dproj.shape, BF16),
                   jax.ShapeDtypeStruct((CONV_PAD, C), F32), jax.ShapeDtypeStruct((1, C), F32)],
        input_output_aliases={4: 0},
        scratch_shapes=[pltpu.VMEM((S + CONV_PAD, LANES), F32), pltpu.VMEM((TIME_BLOCK + CONV_PAD, LANES), F32),
                        pltpu.VMEM((CONV_PAD, LANES), F32),
                        pltpu.VMEM((2, 2, S, LANES), BF16), pltpu.SemaphoreType.DMA((2, 2))],
        compiler_params=_params(("arbitrary",)),
    )(proj, proj, w, dc1, dproj)


def _ffn_fwd_call(u, w, b):
    S, C2 = u.shape
    C = C2 // 2
    width = _pick(C, (2 * LANES, LANES))
    nt = C // width

    def body(ug_ref, uv_ref, wg_ref, wv_ref, bg_ref, bv_ref, f_ref, xg_ref, xv_ref):
        zeros = jnp.zeros((FFN_PAD, LANES), F32)
        for part in range(width // LANES):
            lanes = slice(part * LANES, (part + 1) * LANES)
            xg_ref[0:FFN_PAD, :] = zeros
            xv_ref[0:FFN_PAD, :] = zeros
            xg_ref[FFN_PAD:, :] = ug_ref[:, lanes]
            xv_ref[FFN_PAD:, :] = uv_ref[:, lanes]

            def step(t0, lanes=lanes):
                cg = _conv_block(xg_ref, t0, wg_ref.at[:, lanes], FFN_CONV_WIDTH, FFN_PAD) + bg_ref[:, lanes]
                cv = _conv_block(xv_ref, t0, wv_ref.at[:, lanes], FFN_CONV_WIDTH, FFN_PAD) + bv_ref[:, lanes]
                f_ref[pl.ds(t0, TIME_BLOCK), lanes] = (_gelu(cg) * cv).astype(BF16)

            _time_loop(S, step)

    seq = lambda off: pl.BlockSpec((S, width), lambda i: (0, off + i))
    wsp = lambda off: pl.BlockSpec((FFN_CONV_WIDTH, width), lambda i: (0, off + i))
    bsp = lambda off: pl.BlockSpec((1, width), lambda i: (0, off + i))
    return pl.pallas_call(
        body, name="ffn_conv_geglu", grid=(nt,),
        in_specs=[seq(0), seq(nt), wsp(0), wsp(nt), bsp(0), bsp(nt)],
        out_specs=seq(0), out_shape=jax.ShapeDtypeStruct((S, C), BF16),
        scratch_shapes=[pltpu.VMEM((FFN_PAD + S, LANES), F32)] * 2,
        compiler_params=_params(("parallel",)),
    )(u, u, w, w, b, b)


def _ffn_bwd_call(u, w, b, df):
    S, C2 = u.shape
    C = C2 // 2
    nt = C // LANES

    def body(ug_ref, uv_ref, wg_ref, wv_ref, bg_ref, bv_ref, df_ref,
             du_ref, dwg_ref, dwv_ref, dbg_ref, dbv_ref,
             hg_ref, hv_ref, dg_ref, dv_ref, dwg_acc, dwv_acc, dbg_acc, dbv_acc):
        zeros = jnp.zeros((FFN_PAD, LANES), F32)
        _fill_head(hg_ref, ug_ref, FFN_PAD)
        _fill_head(hv_ref, uv_ref, FFN_PAD)
        dg_ref[S:, :] = zeros
        dv_ref[S:, :] = zeros
        dwg_acc[...] = jnp.zeros_like(dwg_acc)
        dwv_acc[...] = jnp.zeros_like(dwv_acc)
        dbg_acc[...] = jnp.zeros_like(dbg_acc)
        dbv_acc[...] = jnp.zeros_like(dbv_acc)

        def first(t0, xg_ref, xv_ref, x_t0, pad):
            rows = pl.ds(t0, TIME_BLOCK)
            cg = _conv_block(xg_ref, x_t0, wg_ref, FFN_CONV_WIDTH, pad) + bg_ref[...]
            cv = _conv_block(xv_ref, x_t0, wv_ref, FFN_CONV_WIDTH, pad) + bv_ref[...]
            dfb = df_ref[rows, :]
            gelu, gelu_grad = _gelu_and_grad(cg)
            dcg = dfb * cv * gelu_grad
            dcv = dfb * gelu
            dg_ref[rows, :] = dcg
            dv_ref[rows, :] = dcv
            _conv_weight_grad(xg_ref, x_t0, dcg, dwg_acc, FFN_CONV_WIDTH, pad)
            _conv_weight_grad(xv_ref, x_t0, dcv, dwv_acc, FFN_CONV_WIDTH, pad)
            dbg_acc[...] += jnp.sum(dcg, axis=0, keepdims=True)
            dbv_acc[...] += jnp.sum(dcv, axis=0, keepdims=True)

        def second(t0):
            rows = pl.ds(t0, TIME_BLOCK)
            du_ref[0, rows, :] = _conv_transpose_block(dg_ref, t0, wg_ref, FFN_CONV_WIDTH).astype(BF16)
            du_ref[1, rows, :] = _conv_transpose_block(dv_ref, t0, wv_ref, FFN_CONV_WIDTH).astype(BF16)

        first(0, hg_ref, hv_ref, 0, FFN_PAD)
        _time_loop(S, lambda t0: first(t0, ug_ref, uv_ref, t0, 0), skip_first=1)
        _time_loop(S, second)
        dwg_ref[...] = dwg_acc[...]
        dwv_ref[...] = dwv_acc[...]
        dbg_ref[...] = dbg_acc[...]
        dbv_ref[...] = dbv_acc[...]

    seq = lambda off: pl.BlockSpec((S, LANES), lambda i: (0, off + i))
    wsp = lambda off: pl.BlockSpec((FFN_CONV_WIDTH, LANES), lambda i: (0, off + i))
    bsp = lambda off: pl.BlockSpec((1, LANES), lambda i: (0, off + i))
    return pl.pallas_call(
        body, name="ffn_conv_geglu_bwd", grid=(nt,),
        in_specs=[seq(0), seq(nt), wsp(0), wsp(nt), bsp(0), bsp(nt), seq(0)],
        out_specs=[pl.BlockSpec((2, S, LANES), lambda i: (0, 0, i)),
                   pl.BlockSpec((SUBLANES, LANES), lambda i: (0, i)), pl.BlockSpec((SUBLANES, LANES), lambda i: (0, i)),
                   bsp(0), bsp(0)],
        out_shape=[jax.ShapeDtypeStruct((2, S, C), BF16)] + [jax.ShapeDtypeStruct((SUBLANES, C), F32)] * 2
        + [jax.ShapeDtypeStruct((1, C), F32)] * 2,
        scratch_shapes=[pltpu.VMEM((FFN_PAD + TIME_BLOCK, LANES), F32)] * 2 + [pltpu.VMEM((S + FFN_PAD, LANES), F32)] * 2
        + [pltpu.VMEM((SUBLANES, LANES), F32)] * 2
        + [pltpu.VMEM((1, LANES), F32)] * 2,
        compiler_params=_params(("parallel",)),
    )(u, u, w, w, b, b, df)


def _adamw(w_ref, g_ref, m_ref, v_ref, d_ref, mo_ref, vo_ref):
    gv = g_ref[...]
    mn = ADAM_B1 * m_ref[...] + (1.0 - ADAM_B1) * gv
    vn = ADAM_B2 * v_ref[...] + (1.0 - ADAM_B2) * (gv * gv)
    mo_ref[...] = mn
    vo_ref[...] = vn
    m_hat = mn * (1.0 / (1.0 - ADAM_B1 ** ADAM_STEP))
    v_hat = vn * (1.0 / (1.0 - ADAM_B2 ** ADAM_STEP))
    d_ref[...] = -ADAM_LR * (m_hat / (jnp.sqrt(v_hat) + ADAM_EPS) + ADAM_WD * w_ref[...])


def _adamw_call(w, g, m, v, name):
    R, C = w.shape
    tr = _row_tile(R, C)

    def body(w_ref, g_ref, m_ref, v_ref, go_ref, d_ref, mo_ref, vo_ref):
        go_ref[...] = g_ref[...]
        _adamw(w_ref, g_ref, m_ref, v_ref, d_ref, mo_ref, vo_ref)

    spec = pl.BlockSpec((tr, C), lambda i: (i, 0))
    return pl.pallas_call(
        body, name=name, grid=(R // tr,),
        in_specs=[spec] * 4, out_specs=[spec] * 4,
        out_shape=[jax.ShapeDtypeStruct((R, C), F32)] * 4,
        compiler_params=_params(("parallel",)),
    )(w, g, m, v)


def _adamw_small_call(ws, gs, ms, vs):
    n = len(ws)

    def body(*refs):
        w_refs, g_refs, m_refs, v_refs, d_refs, mo_refs, vo_refs = (refs[i * n:(i + 1) * n] for i in range(7))
        for i in range(n):
            _adamw(w_refs[i], g_refs[i], m_refs[i], v_refs[i], d_refs[i], mo_refs[i], vo_refs[i])

    whole = pl.BlockSpec(memory_space=pltpu.VMEM)
    outs = pl.pallas_call(
        body, name="adamw_small",
        in_specs=[whole] * (4 * n), out_specs=[whole] * (3 * n),
        out_shape=[jax.ShapeDtypeStruct(w.shape, F32) for w in ws] * 3,
    )(*ws, *gs, *ms, *vs)
    return outs[:n], outs[n:2 * n], outs[2 * n:]


def _position():
    return lax.axis_index("x"), lax.axis_index("y"), lax.axis_index("c")


def _chip_peers(x, y):
    return [(x, 1 - y), (1 - x, y), (1 - x, 1 - y)]


def _half_rows(ref, core, rows):
    h = rows // 2
    start = pl.multiple_of(core * h, PACKED_ROWS)
    return ref.at[pl.ds(start, h), :] if len(ref.shape) == 2 else ref.at[:, pl.ds(start, h), :]


def _shard_half(ref, shard, core, rows):
    h = rows // 2
    return ref.at[shard, pl.ds(pl.multiple_of(core * h, PACKED_ROWS), h), :]


ANY = pl.BlockSpec(memory_space=pl.ANY)


def _first_hop_copies(srcs, lands):
    x, y, c = _position()
    chip = 2 * x + y
    targets = [(px, py, c) for px, py in _chip_peers(x, y)] + [(x, y, 1 - c)]
    rows = srcs[0].shape[0]
    out = []
    for i, (s, l) in enumerate(zip(srcs, lands)):
        for k, dev in enumerate(targets):
            if i == 0 and k < 3:
                out.append((_half_rows(s, c, rows), _shard_half(l, chip, c, rows), dev, k))
            else:
                out.append((s, l.at[chip], dev, len(targets) * i + k))
    return out


def _second_hop_copies(srcs, lands):
    x, y, c = _position()
    rows = lands[0].shape[1]
    out = []
    for k, (px, py) in enumerate(_chip_peers(x, y)):
        half = _shard_half(lands[0], 2 * px + py, c, rows)
        out.append((half, half, (x, y, 1 - c), k))
    return out


HBM_SPEC = pl.BlockSpec(memory_space=pltpu.HBM)
SEM_SPEC = pl.BlockSpec(memory_space=pltpu.SEMAPHORE)
DATAFLOW = pltpu.SideEffectType.DATAFLOW_SIDE_EFFECTING


def _in_hbm(a):
    return pltpu.with_memory_space_constraint(a, pltpu.HBM)


SIBLING_HANDSHAKE_IDS = dict(grad_exchange_start_w_up=1, grad_assemble_start_others=2, grad_assemble_start_w_in=3)


def _split_start(name, groups, after, carry=None, sibling_only=False):
    spans, arrays = [], []
    for srcs, lands, _, _ in groups:
        spans.append((len(arrays), len(srcs), len(lands)))
        arrays += list(srcs) + list(lands)
    if carry is not None:
        arrays.append(carry)
    na, ng = len(arrays), len(groups)

    def body(*refs):
        sems, token = refs[na + 1:na + 1 + 2 * ng], refs[-1]
        if sibling_only:
            x, y, c = _position()
            barrier = pltpu.get_barrier_semaphore()
            pl.semaphore_signal(barrier, inc=1, device_id=(x, y, 1 - c), device_id_type=MESH)
            pl.semaphore_wait(barrier, 1)
        for g, (_, _, _, copies) in enumerate(groups):
            off, ns, nl = spans[g]
            for src, dst, dev, idx in copies(refs[off:off + ns], refs[off + ns:off + ns + nl]):
                pltpu.make_async_remote_copy(src_ref=src, dst_ref=dst, send_sem=sems[2 * g].at[idx], recv_sem=sems[2 * g + 1].at[idx],
                                             device_id=dev, device_id_type=MESH).start()
        token[...] = jnp.zeros_like(token)

    outs = pl.pallas_call(
        body, name=name,
        in_specs=[HBM_SPEC] * na + [ANY],
        out_specs=[SEM_SPEC] * (2 * ng) + [HBM_SPEC] * na + [pl.BlockSpec(memory_space=pltpu.VMEM)],
        out_shape=[pltpu.SemaphoreType.DMA((n_sems,)) for _, _, n_sems, _ in groups for _ in range(2)]
        + [pltpu.HBM(a.shape, a.dtype) for a in arrays] + [jax.ShapeDtypeStruct((SUBLANES, LANES), F32)],
        input_output_aliases={i: 2 * ng + i for i in range(na)},
        compiler_params=pltpu.CompilerParams(has_side_effects=DATAFLOW,
                                             collective_id=SIBLING_HANDSHAKE_IDS[name] if sibling_only else None),
    )(*[_in_hbm(a) for a in arrays], after)
    started = []
    for g, (off, ns, nl) in enumerate(spans):
        thru = outs[2 * ng + off:2 * ng + off + ns + nl]
        started.append(dict(send=outs[2 * g], recv=outs[2 * g + 1], srcs=list(thru[:ns]), lands=list(thru[ns:]),
                            tile=outs[-1], token=outs[-1][0, 0], carry=None if carry is None else outs[2 * ng + na - 1]))
    return started


def _split_wait(name, started, copies, after):
    n, m = len(started["srcs"]), len(started["lands"])
    after = list(after) if isinstance(after, (list, tuple)) else [after]

    def body(*refs):
        src_refs, land_refs = refs[:n], refs[n:n + m]
        send_sem, recv_sem = refs[n + m], refs[n + m + 1]
        for src, dst, dev, idx in copies(src_refs, land_refs):
            cp = pltpu.make_async_remote_copy(src_ref=src, dst_ref=dst, send_sem=send_sem.at[idx], recv_sem=recv_sem.at[idx],
                                              device_id=dev, device_id_type=MESH)
            cp.wait_send()
            cp.wait_recv()

    arrays = started["srcs"] + started["lands"]
    outs = pl.pallas_call(
        body, name=name,
        in_specs=[HBM_SPEC] * (n + m) + [SEM_SPEC, SEM_SPEC] + [ANY] * len(after),
        out_specs=[HBM_SPEC] * (n + m),
        out_shape=[pltpu.HBM(a.shape, a.dtype) for a in arrays],
        input_output_aliases={i: i for i in range(n + m)},
        compiler_params=pltpu.CompilerParams(has_side_effects=DATAFLOW),
    )(*arrays, started["send"], started["recv"], *after)
    return list(outs)


def _gather_copies(srcs, lands):
    x, y, c = _position()
    chip = 2 * x + y
    targets = [(px, py, c) for px, py in _chip_peers(x, y)] + [(x, y, 1 - c)]
    return [(s, l.at[chip], dev, len(targets) * i + k) for i, (s, l) in enumerate(zip(srcs, lands)) for k, dev in enumerate(targets)]


def _sibling_copies(srcs, lands):
    x, y, c = _position()
    return [(_half_rows(srcs[0], 1 - c, srcs[0].shape[1]), lands[0], (x, y, 1 - c), 0)]


def _sibling_whole_copies(srcs, lands):
    x, y, c = _position()
    return [(srcs[0], lands[0], (x, y, 1 - c), 0)]


def _exchange_copies(srcs, lands):
    x, y, c = _position()
    return [(srcs[0].at[2 * px + py], lands[0].at[k], (px, py, c), k) for k, (px, py) in enumerate(_chip_peers(x, y))]


def _pair_sum_call(grad, recv, chip_core, name):
    _, h, B = recv.shape
    tr = _row_tile(h, B)

    def body(cc_ref, g_ref, r_ref, o_ref, ob_ref):
        s = g_ref[...] + r_ref[...]
        ob_ref[...] = s.astype(BF16)

        @pl.when(pl.program_id(1) == cc_ref[0])
        def _():
            o_ref[...] = s

    g_spec = pl.BlockSpec((None, tr, B), lambda i, q, cc_ref: (q, cc_ref[1] * (h // tr) + i, 0))
    spec = pl.BlockSpec((None, tr, B), lambda i, q, cc_ref: (q, i, 0))
    own_spec = pl.BlockSpec((tr, B), lambda i, q, cc_ref: (i, 0))
    return pl.pallas_call(
        body, name=name,
        grid_spec=pltpu.PrefetchScalarGridSpec(num_scalar_prefetch=1, grid=(h // tr, N_CHIPS), in_specs=[g_spec, spec],
                                               out_specs=[own_spec, spec]),
        out_shape=[jax.ShapeDtypeStruct((h, B), F32), jax.ShapeDtypeStruct(recv.shape, BF16)],
        compiler_params=_params(("parallel", "arbitrary")),
    )(chip_core, grad, recv)


def _chip_sum_call(partial, recv, chip_core, name):
    _, h, B = recv.shape
    tr = _row_tile(h, B)

    def body(cc_ref, p_ref, r_ref, o_ref):
        o_ref[...] = ((p_ref[...] + r_ref[0].astype(F32)) + r_ref[1].astype(F32)) + r_ref[2].astype(F32)

    return pl.pallas_call(
        body, name=name,
        grid_spec=pltpu.PrefetchScalarGridSpec(
            num_scalar_prefetch=1, grid=(h // tr,),
            in_specs=[pl.BlockSpec((tr, B), lambda i, cc_ref: (i, 0)),
                      pl.BlockSpec((3, tr, B), lambda i, cc_ref: (0, i, 0))],
            out_specs=pl.BlockSpec((tr, B), lambda i, cc_ref: (cc_ref[1] * (h // tr) + i, 0))),
        out_shape=jax.ShapeDtypeStruct((2 * h, B), F32),
        compiler_params=_params(("parallel",)),
    )(chip_core, partial, recv)


def _assemble_copies(srcs, lands):
    x, y, c = _position()
    out = []
    for i, land in enumerate(lands):
        half = _half_rows(land, c, land.shape[0])
        out.append((half, half, (x, y, 1 - c), i))
    return out


N_DEVICES = 8


def _allsum_copies(srcs, lands):
    x, y, c = _position()
    me = 4 * x + 2 * y + c
    out = []
    for k in range(1, N_DEVICES):
        peer = (1 - x if k & 4 else x, 1 - y if k & 2 else y, 1 - c if k & 1 else c)
        out.append((srcs[0], lands[0].at[me], peer, k - 1))
    return out


def _ordered_sum_call(mine, landed, me_chip, shapes, sharded_cols):
    rows = mine.shape[0]
    outs = [(s[0], n) if n else s for s, n in zip(shapes, sharded_cols)]

    def body(mc_ref, x_ref, l_ref, *refs):
        acc_ref = refs[-1]
        acc = jnp.where(mc_ref[0] == 0, x_ref[...], l_ref[0])
        for d in range(1, N_DEVICES):
            acc = acc + jnp.where(mc_ref[0] == d, x_ref[...], l_ref[d])
        acc_ref[...] = acc
        first = 0
        for o_ref, (r, c), n in zip(refs[:-1], shapes, sharded_cols):
            per_row = c // LANES

            def unpack(chip, o_ref=o_ref, r=r, n=n, per_row=per_row, first=first):
                for i in range(r):
                    for j in range((n or per_row * LANES) // LANES):
                        src = first + i * per_row + chip * ((n or 0) // LANES) + j
                        o_ref[i:i + 1, j * LANES:(j + 1) * LANES] = acc_ref[src:src + 1, :]

            if n:
                for q in range(N_CHIPS):
                    pl.when(mc_ref[1] == q)(functools.partial(unpack, q))
            else:
                unpack(0)
            first += r * per_row

    results = pl.pallas_call(
        body, name="small_grad_sum",
        in_specs=[pl.BlockSpec(memory_space=pltpu.SMEM), pl.BlockSpec(memory_space=pltpu.VMEM), pl.BlockSpec(memory_space=pltpu.VMEM)],
        out_specs=[pl.BlockSpec(memory_space=pltpu.VMEM)] * len(outs),
        out_shape=[jax.ShapeDtypeStruct(s, F32) for s in outs],
        scratch_shapes=[pltpu.VMEM((rows, LANES), F32)],
    )(me_chip, mine, landed)
    return results


def _pack(arrays):
    flat = jnp.concatenate([a.reshape(-1).astype(F32) for a in arrays])
    rows = -(-flat.shape[0] // LANES)
    rows = -(-rows // SUBLANES) * SUBLANES
    flat = jnp.pad(flat, (0, rows * LANES - flat.shape[0]))
    return flat.reshape(rows, LANES)


def _local_step(xs, target, P, late_weights, on_grad):
    S, D = xs.shape
    qkv_width = 3 * N_HEADS * HEAD_DIM
    glu_col0, gate_col0 = qkv_width, qkv_width + 2 * D
    shard_major = lambda g: g.reshape(N_CHIPS, g.shape[0] // N_CHIPS, g.shape[1])

    h1 = _rms_fwd_call(xs, P["norm_mix_pre"])
    buckets = _bucket_tables()
    bias = _bias_table_call(P["rel_bias"] + 0.0 * h1[0, 0].astype(F32), buckets)
    P = dict(P, **late_weights("in", bias))
    proj = _matmul_ring(h1, P["w_in"], "proj_in")
    dilated = []
    for g in range(1, N_GROUPS):
        dilated += _attn_fwd_call(proj, bias, g)
    a, a_bf, lse = _attn_fwd_call(proj, bias, 0, merge_with=dilated)
    P = dict(P, **late_weights("mix", a_bf))
    y_a = _matmul(a_bf, P["w_attn_out"], "nn", "attn_out")
    c1 = _conv_fwd_call(proj, glu_col0, P["conv_dw_w"], P["conv_dw_b"])
    cact = _ln_silu_call(c1, P["conv_ln_g"], P["conv_ln_b"])
    y_c = _matmul(cact, P["conv_pw_w"], "nn", "conv_pw")
    mixed = _mix_call(proj, gate_col0, P["b_gate"], y_a, y_c)
    out = _matmul(mixed, P["w_out"], "nn", "mix_out")
    x1, h2 = _res1_call(xs, out, P["norm_mix_post"], P["norm_ffn_pre"])
    P = dict(P, **late_weights("up", h2))
    u = _matmul_ring(h2, P["w_up"], "ffn_up")
    f = _ffn_fwd_call(u, P["ffn_conv_w"], P["ffn_conv_b"])
    P = dict(P, **late_weights("down", f))
    yff = _matmul(f, P["w_down"], "nn", "ffn_down")
    loss_tile, dx2, dyff, dg_ffn_post = _loss_call(yff, x1, P["norm_ffn_post"], target)

    G = {}
    G["norm_ffn_post"] = dg_ffn_post
    on_grad("w_down", shard_major(_matmul(f, dyff, "tn", "ffn_down_dw")))
    df = _matmul(dyff, P["w_down"], "nt", "ffn_down_dx")
    du, dwg, dwv, dbg, dbv = _ffn_bwd_call(u, P["ffn_conv_w"], P["ffn_conv_b"], df)
    G["ffn_conv_w"] = jnp.concatenate([dwg[:FFN_CONV_WIDTH], dwv[:FFN_CONV_WIDTH]], axis=1)
    G["ffn_conv_b"] = jnp.concatenate([dbg, dbv], axis=1)
    du = on_grad("w_up", functools.partial(_grad_half_matmul, h2, "ffn_up_dw"), carry=du)
    dh2 = _matmul(du, P["w_up"], "nt", "ffn_up_dx")
    dx1, dout, G["norm_ffn_pre"], G["norm_mix_post"] = _mid_bwd_call(x1, P["norm_ffn_pre"], dh2, dx2, out, P["norm_mix_post"])
    on_grad("w_out", shard_major(_matmul(mixed, dout, "tn", "mix_out_dw")))
    dmixed = _matmul(dout, P["w_out"], "nt", "mix_out_dx")
    dya, dyc, dproj, dba, dbc = _mix_bwd_call(dmixed, proj, gate_col0, P["b_gate"], y_a, y_c)
    G["b_gate"] = jnp.concatenate([dba, dbc], axis=1)
    on_grad("w_attn_out", _matmul(a_bf, dya, "tn", "attn_out_dw", out_shards=True))
    dyc = on_grad("conv_pw_w", shard_major(_matmul(cact, dyc, "tn", "conv_pw_dw")), carry=dyc)
    da = _matmul(dya, P["w_attn_out"], "nt", "attn_out_dx")
    dcact = _matmul(dyc, P["conv_pw_w"], "nt", "conv_pw_dx")
    dc1, G["conv_ln_g"], G["conv_ln_b"] = _ln_silu_bwd_call(c1, P["conv_ln_g"], P["conv_ln_b"], dcact)
    dproj, dw_dw, G["conv_dw_b"] = _conv_bwd_call(proj, glu_col0, P["conv_dw_w"], dc1, dproj)
    G["conv_dw_w"] = dw_dw[:CONV_WIDTH]
    dbs = []
    for g in range(N_GROUPS):
        dproj, db = _attn_bwd_call(proj, bias, a, da, lse, g, dproj)
        dbs.append(db)
    G["rel_bias"] = _bias_grad_call(jnp.concatenate(dbs, axis=0), buckets)
    dproj = on_grad("w_in", functools.partial(_grad_half_matmul, h1, "proj_in_dw"), carry=dproj)
    dh1 = _matmul(dproj, P["w_in"], "nt", "proj_in_dx")
    dh1 = on_grad(None, None, carry=dh1)
    grad_x, G["norm_mix_pre"] = _in_bwd_call(xs, P["norm_mix_pre"], dh1, dx1)
    return loss_tile, grad_x, G


def kernel(x, w_in, b_gate, rel_bias, w_attn_out, conv_dw_w, conv_dw_b, conv_ln_g, conv_ln_b, conv_pw_w, w_out, norm_mix_pre, norm_mix_post, norm_ffn_pre, norm_ffn_post, w_up, ffn_conv_w, ffn_conv_b, w_down, loss_target, m_w_in, m_b_gate, m_rel_bias, m_w_attn_out, m_conv_dw_w, m_conv_dw_b, m_conv_ln_g, m_conv_ln_b, m_conv_pw_w, m_w_out, m_norm_mix_pre, m_norm_mix_post, m_norm_ffn_pre, m_norm_ffn_post, m_w_up, m_ffn_conv_w, m_ffn_conv_b, m_w_down, v_w_in, v_b_gate, v_rel_bias, v_w_attn_out, v_conv_dw_w, v_conv_dw_b, v_conv_ln_g, v_conv_ln_b, v_conv_pw_w, v_w_out, v_norm_mix_pre, v_norm_mix_post, v_norm_ffn_pre, v_norm_ffn_post, v_w_up, v_ffn_conv_w, v_ffn_conv_b, v_w_down):
    weights = dict(w_in=w_in, b_gate=b_gate, rel_bias=rel_bias, w_attn_out=w_attn_out, conv_dw_w=conv_dw_w, conv_dw_b=conv_dw_b,
                   conv_ln_g=conv_ln_g, conv_ln_b=conv_ln_b, conv_pw_w=conv_pw_w, w_out=w_out, norm_mix_pre=norm_mix_pre,
                   norm_mix_post=norm_mix_post, norm_ffn_pre=norm_ffn_pre, norm_ffn_post=norm_ffn_post, w_up=w_up,
                   ffn_conv_w=ffn_conv_w, ffn_conv_b=ffn_conv_b, w_down=w_down)
    m_in = dict(w_in=m_w_in, b_gate=m_b_gate, rel_bias=m_rel_bias, w_attn_out=m_w_attn_out, conv_dw_w=m_conv_dw_w,
                conv_dw_b=m_conv_dw_b, conv_ln_g=m_conv_ln_g, conv_ln_b=m_conv_ln_b, conv_pw_w=m_conv_pw_w, w_out=m_w_out,
                norm_mix_pre=m_norm_mix_pre, norm_mix_post=m_norm_mix_post, norm_ffn_pre=m_norm_ffn_pre,
                norm_ffn_post=m_norm_ffn_post, w_up=m_w_up, ffn_conv_w=m_ffn_conv_w, ffn_conv_b=m_ffn_conv_b, w_down=m_w_down)
    v_in = dict(w_in=v_w_in, b_gate=v_b_gate, rel_bias=v_rel_bias, w_attn_out=v_w_attn_out, conv_dw_w=v_conv_dw_w,
                conv_dw_b=v_conv_dw_b, conv_ln_g=v_conv_ln_g, conv_ln_b=v_conv_ln_b, conv_pw_w=v_conv_pw_w, w_out=v_w_out,
                norm_mix_pre=v_norm_mix_pre, norm_mix_post=v_norm_mix_post, norm_ffn_pre=v_norm_ffn_pre,
                norm_ffn_post=v_norm_ffn_post, w_up=v_w_up, ffn_conv_w=v_ffn_conv_w, ffn_conv_b=v_ffn_conv_b, w_down=v_w_down)
    names = list(weights)
    xi, yi, ci = _position()
    chip = 2 * xi + yi
    core_arr = jnp.reshape(ci, (1,)).astype(jnp.int32)

    xs = x[0]
    target = loss_target[0]
    S, D = xs.shape

    big = ["w_in", "w_attn_out", "conv_pw_w", "w_out", "w_up", "w_down"]
    row_sharded = ("conv_pw_w", "w_out", "w_down")
    natural = lambda k, g: g.reshape(-1, g.shape[2]) if k in row_sharded else g
    first_srcs = [w_in[0].astype(BF16), conv_dw_w[0], ffn_conv_w[0]]
    first_lands = [lax.empty((N_CHIPS,) + s.shape, s.dtype) for s in first_srcs]
    (first_hop,) = _split_start("gather_in_start", [(first_srcs, first_lands, 4 * len(first_srcs), _first_hop_copies)], core_arr)
    launched = first_hop["token"]
    late_sets = dict(mix=["w_attn_out", "conv_pw_w", "w_out"], up=["w_up"], down=["w_down"])
    late_groups = []
    for keys in late_sets.values():
        srcs = [(weights[k][0] + launched).astype(BF16) for k in keys]
        late_groups.append((srcs, [lax.empty((N_CHIPS,) + s.shape, BF16) for s in srcs], 4 * len(keys), _gather_copies))
    started = {}

    def late_weights(tag, after):
        if tag == "in":
            casts = [s for srcs, _, _, _ in late_groups for s in srcs]
            w_in_halves, dw4, fc4 = _split_wait("gather_in_wait", first_hop, _first_hop_copies, [after] + casts)[len(first_srcs):]
            second_hop, *late = _split_start("gather_in_pass_start", [([], [w_in_halves], 3, _second_hop_copies)] + late_groups, dw4)
            started.update(zip(late_sets, late))
            (w_in_full,) = _split_wait("gather_in_pass_wait", second_hop, _second_hop_copies, second_hop["tile"])
            return dict(w_in=w_in_full, conv_dw_w=jnp.concatenate(list(dw4), axis=1), ffn_conv_w=jnp.concatenate(list(fc4), axis=1))
        landed = _split_wait(f"gather_{tag}_wait", started[tag], _gather_copies, after)[len(late_sets[tag]):]
        return {k: natural(k, g) for k, g in zip(late_sets[tag], landed)}

    chip_core = jnp.stack([chip, ci]).astype(jnp.int32)
    exchanging, pending, second_half = {}, {}, {}

    held = []

    def launch(tag, after, carry=None):
        keys, groups, partial = [], [], {}
        for k in list(exchanging):
            st, copies = exchanging.pop(k)
            gk, r1 = _split_wait(f"sibling_exchange_wait_{k}", st, copies, after)
            if k in second_half:
                partial[k], s16 = second_half.pop(k)(init=r1)
            else:
                partial[k], s16 = _pair_sum_call(gk, r1, chip_core, f"pair_sum_{k}")
            keys.append(k)
            groups.append(([s16], [lax.empty((3,) + s16.shape[1:], BF16)], 3, _exchange_copies))
        fresh = [(k, copies) for k, _, copies in held]
        for _, g3, copies in held:
            rows = g3.shape[1] // 2 if copies is _sibling_copies else g3.shape[1]
            groups.append(([g3], [lax.empty((N_CHIPS, rows, g3.shape[2]), F32)], 1, copies))
        held.clear()
        begun = _split_start(f"grad_exchange_start_{tag}", groups, core_arr, carry, sibling_only=not keys)
        for k, st in zip(keys, begun):
            pending[k] = (partial[k], st)
        for (k, copies), st in zip(fresh, begun[len(keys):]):
            exchanging[k] = (st, copies)
        return begun[0]["carry"]

    others = [k for k in big if k != "w_in"]
    assembling = {}

    def on_grad(k, g, carry=None):
        if k is None:
            carried = launch("last", carry[:SUBLANES, :LANES], carry)
            assembling["others"] = assemble_start(others, carried, "others", carried)
            return assembling["others"]["carry"]
        if callable(g):
            theirs = g(jnp.stack([chip, 1 - ci]).astype(jnp.int32), carry)
            held.append((k, theirs, _sibling_whole_copies))
            carried = launch(k, theirs[0, :SUBLANES, :LANES], carry)
            second_half[k] = functools.partial(g, chip_core, carried)
            return carried
        held.append((k, g, _sibling_copies))
        if k in ("w_down", "w_out", "w_attn_out"):
            return carry
        return launch(k, g[0, :SUBLANES, :LANES], carry)

    def assemble_start(keys, after, tag, carry=None):
        halves = []
        for k in keys:
            s32, st = pending[k]
            recv2 = _split_wait(f"chip_exchange_wait_{k}", st, _exchange_copies, after)[1]
            halves.append(_chip_sum_call(s32, recv2, chip_core, f"chip_sum_{k}"))
        (st,) = _split_start(f"grad_assemble_start_{tag}", [([], halves, len(halves), _assemble_copies)], core_arr, carry,
                             sibling_only=True)
        return st

    def assemble_wait(keys, st, after, tag):
        return dict(zip(keys, _split_wait(f"grad_assemble_wait_{tag}", st, _assemble_copies, after)))

    P = dict(b_gate=b_gate, rel_bias=rel_bias, conv_dw_b=conv_dw_b, conv_ln_g=conv_ln_g, conv_ln_b=conv_ln_b,
             norm_mix_pre=norm_mix_pre + launched, norm_mix_post=norm_mix_post, norm_ffn_pre=norm_ffn_pre,
             norm_ffn_post=norm_ffn_post, ffn_conv_b=ffn_conv_b)
    loss_tile, grad_x, G = _local_step(xs, target, P, late_weights, on_grad)

    small = [k for k in names if k not in big]
    packed = _pack([loss_tile[:1]] + [G[k] for k in small])
    (allsum,) = _split_start("small_grad_allsum_start",
                             [([packed], [jnp.zeros((N_DEVICES,) + packed.shape, F32)], N_DEVICES - 1, _allsum_copies)], core_arr)

    reduced, grads, deltas, new_m, new_v = {}, {}, {}, {}, {}

    def update(keys):
        for k in keys:
            gk, d, mn, vn = _adamw_call(weights[k][0], reduced[k], m_in[k][0], v_in[k][0], f"adamw_{k}")
            grads[k], deltas[k], new_m[k], new_v[k] = gk[None], d[None], mn[None], vn[None]

    reduced.update(assemble_wait(others, assembling["others"], [allsum["tile"], grad_x], "others"))
    update(others)
    assembling["w_in"] = assemble_start(["w_in"], [deltas[k] for k in others], "w_in")

    me_chip = jnp.stack([4 * xi + 2 * yi + ci, chip]).astype(jnp.int32)
    mine, landed = _split_wait("small_grad_allsum_wait", allsum, _allsum_copies, assembling["w_in"]["tile"])
    piece_shapes = [(1, LANES)] + [(G[k].size // LANES, LANES) if k == "rel_bias" else G[k].shape for k in small]
    piece_cols = [0] + [weights[k].shape[2] if k in ("conv_dw_w", "ffn_conv_w") else 0 for k in small]
    loss_row, *summed = _ordered_sum_call(mine, landed, me_chip, piece_shapes, piece_cols)
    loss = loss_row[0, 0]
    for k, gsum in zip(small, summed):
        grads[k] = gsum.reshape(weights[k].shape)
    ds, mns, vns = _adamw_small_call([weights[k] for k in small], [grads[k] for k in small],
                                     [m_in[k] for k in small], [v_in[k] for k in small])
    deltas.update(zip(small, ds))
    new_m.update(zip(small, mns))
    new_v.update(zip(small, vns))
    reduced.update(assemble_wait(["w_in"], assembling["w_in"], list(ds), "w_in"))
    update(["w_in"])

    return (loss, grad_x[None], *[grads[k] for k in names], *[deltas[k] for k in names],
            *[new_m[k] for k in names], *[new_v[k] for k in names])
```

```python
import functools
import math

import jax
import jax.numpy as jnp
import numpy as np
from jax import lax
from jax.experimental import pallas as pl
from jax.experimental.pallas import tpu as pltpu

F32 = jnp.float32
BF16 = jnp.bfloat16
MESH = pl.DeviceIdType.MESH

HEAD_DIM = 128
HEADS_PER_GROUP = 4
DILATED_PATTERNS = ((128, 1), (512, 4), (2048, 16))
N_GROUPS = 3
N_HEADS = N_GROUPS * HEADS_PER_GROUP
SPAN = 128
GROUP_WIDTH = HEADS_PER_GROUP * HEAD_DIM
CONV_WIDTH = 31
FFN_CONV_WIDTH = 3
N_BUCKETS = 32
MAX_DISTANCE = 2048
RMS_EPS = 1e-6
LN_EPS = 1e-5
NEG_INF = -1e30
ADAM_LR = 0.001
ADAM_B1 = 0.9
ADAM_B2 = 0.999
ADAM_EPS = 1e-08
ADAM_WD = 0.01
ADAM_STEP = 10

LANES = 128
SUBLANES = 8
PACKED_ROWS = 16
ROW_TILE = 512
GATE_ROWS, GATE_COLS = 512, 512
TIME_BLOCK = 128
CONV_PAD = 32
FFN_PAD = 8
VMEM_LIMIT = 56 << 20


def _params(sem=None, vmem=None):
    kw = {}
    if sem is not None:
        kw["dimension_semantics"] = sem
    if vmem is not None:
        kw["vmem_limit_bytes"] = vmem
    return pltpu.CompilerParams(**kw)


def _pick(n, cands):
    for c in cands:
        if n % c == 0:
            return c
    return n


ELEMENTWISE_TILE_BYTES = 3 << 19


def _row_tile(rows, cols):
    for align in (16, SUBLANES):
        fits = [t for t in range(align, rows + 1, align) if rows % t == 0 and t * cols * 4 <= ELEMENTWISE_TILE_BYTES]
        if fits:
            return max(fits)
    return SUBLANES


N_CHIPS = 4
M_TILES = (1024, 1408, 512, 256, 128)
N_TILES = (1024, 512, 1408, 256, 128)
K_TILES = (2176, 2048, 1408, 1024, 512, 256, 128)


def _matmul(a, b, mode, name, out_shards=False, tm=None):
    assert a.dtype == BF16 and b.dtype == BF16, (name, a.dtype, b.dtype)
    b3 = b.ndim == 3
    tn = tk = None
    halves = None
    if mode == "nn":
        M, K = a.shape
        N = b.shape[-1] * (N_CHIPS if b3 else 1)
        tn = b.shape[-1] if b3 else None
    elif mode == "nt":
        if a.ndim == 3:
            halves = a.shape[2]
        M, K = a.shape[-2], a.shape[-1] * (a.shape[0] if a.ndim == 3 else 1)
        N = b.shape[-2]
        tk = b.shape[-1] if b3 else None
    else:
        if b3:
            halves = b.shape[2]
        K, M = a.shape
        N = b.shape[-1] * (b.shape[0] if b3 else 1)
        tn = N // N_CHIPS if out_shards else None
    tm = tm or _pick(M, M_TILES)
    tn = tn or _pick(N, N_TILES)
    tk = tk or _pick(K, K_TILES)
    nk = K // tk
    dn = {"nn": (((1,), (0,)), ((), ())), "nt": (((1,), (1,)), ((), ())), "tn": (((0,), (0,)), ((), ()))}[mode]

    def body(a_ref, b_ref, o_ref):
        if nk == 1:
            o_ref[...] = lax.dot_general(a_ref[...], b_ref[...], dn, preferred_element_type=F32)
        else:
            @pl.when(pl.program_id(2) == 0)
            def _():
                o_ref[...] = jnp.zeros_like(o_ref)

            o_ref[...] += lax.dot_general(a_ref[...], b_ref[...], dn, preferred_element_type=F32)

    if mode == "tn":
        a_spec = pl.BlockSpec((tk, tm), lambda i, j, k: (k, i))
    elif halves:
        per = halves // tk
        a_spec = pl.BlockSpec((None, tm, tk), lambda i, j, k: (k // per, i, k % per))
    else:
        a_spec = pl.BlockSpec((tm, tk), lambda i, j, k: (i, k))
    if mode == "nn":
        b_spec = pl.BlockSpec((None, tk, tn), lambda i, j, k: (j, k, 0)) if b3 else pl.BlockSpec((tk, tn), lambda i, j, k: (k, j))
    elif mode == "nt":
        b_spec = pl.BlockSpec((None, tn, tk), lambda i, j, k: (k, j, 0)) if b3 else pl.BlockSpec((tn, tk), lambda i, j, k: (j, k))
    elif halves:
        per = halves // tn
        b_spec = pl.BlockSpec((None, tk, tn), lambda i, j, k: (j // per, k, j % per))
    else:
        b_spec = pl.BlockSpec((tk, tn), lambda i, j, k: (k, j))
    if out_shards:
        out_spec = pl.BlockSpec((None, tm, tn), lambda i, j, k: (j, i, 0))
        out_shape = jax.ShapeDtypeStruct((N_CHIPS, M, tn), F32)
    else:
        out_spec = pl.BlockSpec((tm, tn), lambda i, j, k: (i, j))
        out_shape = jax.ShapeDtypeStruct((M, N), F32)
    return pl.pallas_call(
        body, name=name, grid=(M // tm, N // tn, nk),
        in_specs=[a_spec, b_spec], out_specs=out_spec, out_shape=out_shape,
        compiler_params=_params(("parallel", "parallel", "arbitrary"), VMEM_LIMIT),
    )(a, b)


def _grad_half_matmul(a, name, chip_half, b, init=None):
    K, M = a.shape
    parts = b.ndim == 3
    N = b.shape[-1] * (b.shape[0] if parts else 1)
    h, tn = M // 2, N // N_CHIPS
    summed = init is not None
    dn = (((0,), (0,)), ((), ()))

    def body(ch_ref, a_ref, b_ref, *rest):
        product = lax.dot_general(a_ref[...], b_ref[...], dn, preferred_element_type=F32)
        if not summed:
            rest[0][...] = product
            return
        init_ref, own_ref, sum16_ref = rest
        total = product + init_ref[...]
        sum16_ref[...] = total.astype(BF16)

        @pl.when(pl.program_id(0) == ch_ref[0])
        def _():
            own_ref[...] = total

    if parts:
        per = b.shape[2] // tn
        b_spec = pl.BlockSpec((None, K, tn), lambda j, ch_ref: (j // per, 0, j % per))
    else:
        b_spec = pl.BlockSpec((K, tn), lambda j, ch_ref: (0, j))
    shard_spec = pl.BlockSpec((None, h, tn), lambda j, ch_ref: (j, 0, 0))
    own_spec = pl.BlockSpec((h, tn), lambda j, ch_ref: (0, 0))
    shape = (N_CHIPS, h, tn)
    return pl.pallas_call(
        body, name=name + ("_mine" if summed else "_theirs"),
        grid_spec=pltpu.PrefetchScalarGridSpec(
            num_scalar_prefetch=1, grid=(N_CHIPS,),
            in_specs=[pl.BlockSpec((K, h), lambda j, ch_ref: (0, ch_ref[1])), b_spec] + [shard_spec] * summed,
            out_specs=[own_spec, shard_spec] if summed else shard_spec),
        out_shape=[jax.ShapeDtypeStruct((h, tn), F32), jax.ShapeDtypeStruct(shape, BF16)] if summed
        else jax.ShapeDtypeStruct(shape, F32),
        compiler_params=_params(("arbitrary",), VMEM_LIMIT),
    )(chip_half, a, b, *([init] if summed else []))


def _rms(x, g):
    r = lax.rsqrt(jnp.mean(x * x, axis=-1, keepdims=True) + RMS_EPS)
    return x * r * g


def _rms_bwd(x, g, dy):
    r = lax.rsqrt(jnp.mean(x * x, axis=-1, keepdims=True) + RMS_EPS)
    n = x * r
    dn = dy * g
    dx = r * (dn - n * jnp.mean(dn * n, axis=-1, keepdims=True))
    return dx, jnp.sum(dy * n, axis=0, keepdims=True)


def _sigmoid(x):
    return 1.0 / (1.0 + jnp.exp(-x))


_GELU_C = math.sqrt(2.0 / math.pi)


def _gelu(x):
    return 0.5 * x * (1.0 + jnp.tanh(_GELU_C * (x + 0.044715 * x * x * x)))


def _gelu_and_grad(x):
    x2 = x * x
    t = jnp.tanh(_GELU_C * x * (1.0 + 0.044715 * x2))
    half = 0.5 * (1.0 + t)
    return x * half, half + (0.5 * _GELU_C) * x * (1.0 - t * t) * (1.0 + (3.0 * 0.044715) * x2)


def _row_spec(width, col_block=0):
    return pl.BlockSpec((ROW_TILE, width), lambda i: (i, col_block))


def _vec_spec(width, col_block=0):
    return pl.BlockSpec((1, width), lambda i: (0, col_block))


def _accumulate(ref, part):
    @pl.when(pl.program_id(0) == 0)
    def _():
        ref[...] = part

    @pl.when(pl.program_id(0) > 0)
    def _():
        ref[...] += part


def _rms_fwd_call(x, g):
    S, D = x.shape

    def body(x_ref, g_ref, h_ref):
        h_ref[...] = _rms(x_ref[...], g_ref[...]).astype(BF16)

    return pl.pallas_call(
        body, name="rms_mix_pre", grid=(S // ROW_TILE,),
        in_specs=[_row_spec(D), _vec_spec(D)], out_specs=_row_spec(D),
        out_shape=jax.ShapeDtypeStruct((S, D), BF16),
        compiler_params=_params(("parallel",)),
    )(x, g)


def _ln_silu_call(c1, g, b):
    S, C = c1.shape

    def body(c_ref, g_ref, b_ref, o_ref):
        xv = c_ref[...]
        mu = jnp.mean(xv, axis=-1, keepdims=True)
        xc = xv - mu
        var = jnp.mean(xc * xc, axis=-1, keepdims=True)
        z = xc * lax.rsqrt(var + LN_EPS) * g_ref[...] + b_ref[...]
        o_ref[...] = (z * _sigmoid(z)).astype(BF16)

    return pl.pallas_call(
        body, name="conv_ln_silu", grid=(S // ROW_TILE,),
        in_specs=[_row_spec(C), _vec_spec(C), _vec_spec(C)], out_specs=_row_spec(C),
        out_shape=jax.ShapeDtypeStruct((S, C), BF16),
        compiler_params=_params(("parallel",)),
    )(c1, g, b)


def _ln_silu_bwd_call(c1, g, b, dc):
    S, C = c1.shape

    def body(c_ref, g_ref, b_ref, dc_ref, dx_ref, dg_ref, db_ref):
        xv = c_ref[...]
        mu = jnp.mean(xv, axis=-1, keepdims=True)
        xc = xv - mu
        rs = lax.rsqrt(jnp.mean(xc * xc, axis=-1, keepdims=True) + LN_EPS)
        xh = xc * rs
        z = xh * g_ref[...] + b_ref[...]
        sg = _sigmoid(z)
        dz = dc_ref[...] * (sg * (1.0 + z * (1.0 - sg)))
        dxh = dz * g_ref[...]
        dx_ref[...] = rs * (dxh - jnp.mean(dxh, axis=-1, keepdims=True) - xh * jnp.mean(dxh * xh, axis=-1, keepdims=True))
        _accumulate(dg_ref, jnp.sum(dz * xh, axis=0, keepdims=True))
        _accumulate(db_ref, jnp.sum(dz, axis=0, keepdims=True))

    return pl.pallas_call(
        body, name="conv_ln_silu_bwd", grid=(S // ROW_TILE,),
        in_specs=[_row_spec(C), _vec_spec(C), _vec_spec(C), _row_spec(C)],
        out_specs=[_row_spec(C), _vec_spec(C), _vec_spec(C)],
        out_shape=[jax.ShapeDtypeStruct((S, C), F32), jax.ShapeDtypeStruct((1, C), F32), jax.ShapeDtypeStruct((1, C), F32)],
        compiler_params=_params(("arbitrary",)),
    )(c1, g, b, dc)


def _mix_call(proj, gate_col0, b_gate, y_a, y_c):
    S, D = y_a.shape
    w = GATE_COLS
    nc = D // w
    ga0, gc0 = gate_col0 // w, (gate_col0 + D) // w

    def body(ga_ref, gc_ref, ba_ref, bc_ref, ya_ref, yc_ref, o_ref):
        o_ref[...] = (_sigmoid(ga_ref[...] + ba_ref[...]) * ya_ref[...]
                      + _sigmoid(gc_ref[...] + bc_ref[...]) * yc_ref[...]).astype(BF16)

    tile = lambda off: pl.BlockSpec((GATE_ROWS, w), lambda i, j: (i, off + j))
    vec = lambda off: pl.BlockSpec((1, w), lambda i, j: (0, off + j))
    return pl.pallas_call(
        body, name="gate_mix", grid=(S // GATE_ROWS, nc),
        in_specs=[tile(ga0), tile(gc0), vec(0), vec(nc), tile(0), tile(0)],
        out_specs=tile(0), out_shape=jax.ShapeDtypeStruct((S, D), BF16),
        compiler_params=_params(("parallel", "parallel")),
    )(proj, proj, b_gate, b_gate, y_a, y_c)


def _window_stores(stage_ref, slot, dst_ref, rows, cols, sems):
    width = stage_ref.shape[-1]
    return [pltpu.make_async_copy(stage_ref.at[slot, p], dst_ref.at[rows, pl.ds(pl.multiple_of(c, LANES), width)], sems.at[slot, p])
            for p, c in enumerate(cols)]


def _staged_window_stores(stage_ref, dst_ref, sems, step, n_steps, rows, cols, fill):
    slot = step % 2
    copies = lambda s: _window_stores(stage_ref, s, dst_ref, rows, cols, sems)

    @pl.when(step >= 2)
    def _():
        for cp in copies(slot):
            cp.wait()

    fill(slot)
    for cp in copies(slot):
        cp.start()

    @pl.when(step == n_steps - 1)
    def _():
        for s in ([slot, 1 - slot] if n_steps > 1 else [slot]):
            for cp in copies(s):
                cp.wait()


def _mix_bwd_call(dmixed, proj, gate_col0, b_gate, y_a, y_c):
    S, D = y_a.shape
    w = GATE_COLS
    nc = D // w
    nr = S // GATE_ROWS
    ga0, gc0 = gate_col0 // w, (gate_col0 + D) // w

    def body(dm_ref, ga_ref, gc_ref, ba_ref, bc_ref, ya_ref, yc_ref, dya_ref, dyc_ref, dproj_ref, dba_ref, dbc_ref,
             stage_ref, sems):
        j, i = pl.program_id(0), pl.program_id(1)
        dm = dm_ref[...]
        sa = _sigmoid(ga_ref[...] + ba_ref[...])
        sc = _sigmoid(gc_ref[...] + bc_ref[...])
        dya_ref[...] = (dm * sa).astype(BF16)
        dyc_ref[...] = (dm * sc).astype(BF16)
        dga = dm * ya_ref[...] * sa * (1.0 - sa)
        dgc = dm * yc_ref[...] * sc * (1.0 - sc)

        def fill(slot):
            stage_ref[slot, 0] = dga.astype(BF16)
            stage_ref[slot, 1] = dgc.astype(BF16)

        rows = pl.ds(pl.multiple_of(i * GATE_ROWS, GATE_ROWS), GATE_ROWS)
        _staged_window_stores(stage_ref, dproj_ref, sems, j * nr + i, nc * nr, rows,
                              [gate_col0 + j * w, gate_col0 + D + j * w], fill)
        pa = jnp.sum(dga, axis=0, keepdims=True)
        pc = jnp.sum(dgc, axis=0, keepdims=True)

        @pl.when(i == 0)
        def _():
            dba_ref[...] = pa
            dbc_ref[...] = pc

        @pl.when(i > 0)
        def _():
            dba_ref[...] += pa
            dbc_ref[...] += pc

    tile = lambda off: pl.BlockSpec((GATE_ROWS, w), lambda j, i: (i, off + j))
    vec = lambda off: pl.BlockSpec((1, w), lambda j, i: (0, off + j))
    return pl.pallas_call(
        body, name="gate_mix_bwd", grid=(nc, nr),
        in_specs=[tile(0), tile(ga0), tile(gc0), vec(0), vec(nc), tile(0), tile(0)],
        out_specs=[tile(0), tile(0), ANY, vec(0), vec(0)],
        out_shape=[jax.ShapeDtypeStruct((S, D), BF16)] * 2 + [jax.ShapeDtypeStruct((S, proj.shape[1]), BF16)] + [
                   jax.ShapeDtypeStruct((1, D), F32), jax.ShapeDtypeStruct((1, D), F32)],
        scratch_shapes=[pltpu.VMEM((2, 2, GATE_ROWS, w), BF16), pltpu.SemaphoreType.DMA((2, 2))],
        compiler_params=_params(("arbitrary", "arbitrary")),
    )(dmixed, proj, proj, b_gate, b_gate, y_a, y_c)


def _res1_call(x, out, g_post, g_pre):
    S, D = x.shape

    def body(x_ref, o_ref, gp_ref, gq_ref, x1_ref, h2_ref):
        x1 = x_ref[...] + _rms(o_ref[...], gp_ref[...])
        x1_ref[...] = x1
        h2_ref[...] = _rms(x1, gq_ref[...]).astype(BF16)

    return pl.pallas_call(
        body, name="residual_mix", grid=(S // ROW_TILE,),
        in_specs=[_row_spec(D), _row_spec(D), _vec_spec(D), _vec_spec(D)],
        out_specs=[_row_spec(D), _row_spec(D)],
        out_shape=[jax.ShapeDtypeStruct((S, D), F32), jax.ShapeDtypeStruct((S, D), BF16)],
        compiler_params=_params(("parallel",)),
    )(x, out, g_post, g_pre)


def _ffn_down_loss_call(f, w_down, x1, g_post, target):
    S, K = f.shape
    D = w_down.shape[1]
    tk = _pick(K, K_TILES)
    nk = K // tk

    def body(f_ref, w_ref, x1_ref, g_ref, t_ref, loss_ref, dx_ref, dy_ref, dg_ref, y_ref):
        k = pl.program_id(1)
        product = lax.dot_general(f_ref[...], w_ref[...], (((1,), (0,)), ((), ())), preferred_element_type=F32)

        @pl.when(k == 0)
        def _():
            y_ref[...] = product

        @pl.when(k > 0)
        def _():
            y_ref[...] += product

        @pl.when(k == nk - 1)
        def _():
            yv, gv = y_ref[...], g_ref[...]
            err = x1_ref[...] + _rms(yv, gv) - t_ref[...]
            dx2 = err * (1.0 / D)
            dx_ref[...] = dx2
            dy, dg = _rms_bwd(yv, gv, dx2)
            dy_ref[...] = dy.astype(BF16)
            _accumulate(dg_ref, dg)
            part = 0.5 * jnp.sum(jnp.mean(err * err, axis=-1, keepdims=True), axis=0, keepdims=True)
            _accumulate(loss_ref, jnp.broadcast_to(part, (SUBLANES, LANES)))

    rows = pl.BlockSpec((ROW_TILE, D), lambda i, k: (i, 0))
    vec = pl.BlockSpec((1, D), lambda i, k: (0, 0))
    return pl.pallas_call(
        body, name="ffn_down_loss", grid=(S // ROW_TILE, nk),
        in_specs=[pl.BlockSpec((ROW_TILE, tk), lambda i, k: (i, k)), pl.BlockSpec((tk, D), lambda i, k: (k, 0)), rows, vec, rows],
        out_specs=[pl.BlockSpec((SUBLANES, LANES), lambda i, k: (0, 0)), rows, rows, vec],
        out_shape=[jax.ShapeDtypeStruct((SUBLANES, LANES), F32), jax.ShapeDtypeStruct((S, D), F32),
                   jax.ShapeDtypeStruct((S, D), BF16), jax.ShapeDtypeStruct((1, D), F32)],
        scratch_shapes=[pltpu.VMEM((ROW_TILE, D), F32)],
        compiler_params=_params(("arbitrary", "arbitrary"), VMEM_LIMIT),
    )(f, w_down, x1, g_post, target)


def _mid_bwd_call(x1, g_pre, dh2, dx2, out, g_post):
    S, D = x1.shape

    def body(x1_ref, gq_ref, dh_ref, dx2_ref, o_ref, gp_ref, dx1_ref, do_ref, dgq_ref, dgp_ref):
        d, dgq = _rms_bwd(x1_ref[...], gq_ref[...], dh_ref[...])
        dx1 = dx2_ref[...] + d
        dx1_ref[...] = dx1
        do, dgp = _rms_bwd(o_ref[...], gp_ref[...], dx1)
        do_ref[...] = do.astype(BF16)
        _accumulate(dgq_ref, dgq)
        _accumulate(dgp_ref, dgp)

    return pl.pallas_call(
        body, name="residual_mix_bwd", grid=(S // ROW_TILE,),
        in_specs=[_row_spec(D), _vec_spec(D), _row_spec(D), _row_spec(D), _row_spec(D), _vec_spec(D)],
        out_specs=[_row_spec(D), _row_spec(D), _vec_spec(D), _vec_spec(D)],
        out_shape=[jax.ShapeDtypeStruct((S, D), F32), jax.ShapeDtypeStruct((S, D), BF16)] + [jax.ShapeDtypeStruct((1, D), F32)] * 2,
        compiler_params=_params(("arbitrary",)),
    )(x1, g_pre, dh2, dx2, out, g_post)


def _in_bwd_call(x, g, dh1, dx1):
    S, D = x.shape

    def body(x_ref, g_ref, dh_ref, dx1_ref, gx_ref, dg_ref):
        d, dg = _rms_bwd(x_ref[...], g_ref[...], dh_ref[...])
        gx_ref[...] = dx1_ref[...] + d
        _accumulate(dg_ref, dg)

    return pl.pallas_call(
        body, name="rms_mix_pre_bwd", grid=(S // ROW_TILE,),
        in_specs=[_row_spec(D), _vec_spec(D), _row_spec(D), _row_spec(D)],
        out_specs=[_row_spec(D), _vec_spec(D)],
        out_shape=[jax.ShapeDtypeStruct((S, D), F32), jax.ShapeDtypeStruct((1, D), F32)],
        compiler_params=_params(("arbitrary",)),
    )(x, g, dh1, dx1)


def _bucket_table(dilation):
    qi = np.arange(SPAN)[:, None]
    ki = np.arange(2 * SPAN)[None, :]
    dist = np.maximum(qi + SPAN - ki, 0) * dilation
    max_exact = N_BUCKETS // 2
    d = np.maximum(dist, 1).astype(np.float64)
    large = max_exact + (np.log(d / max_exact) / math.log(MAX_DISTANCE / max_exact) * (N_BUCKETS - max_exact)).astype(np.int32)
    large = np.minimum(large, N_BUCKETS - 1)
    return np.where(dist < max_exact, dist, large).astype(np.int32)


def _bucket_tables():
    return jnp.asarray(np.stack([_bucket_table(r) for _, r in DILATED_PATTERNS]))


def _bias_table_call(rel_bias, buckets):
    def body(rb_ref, bk_ref, o_ref):
        for h in range(N_HEADS):
            bk = bk_ref[h // HEADS_PER_GROUP]

            def step(b, acc):
                return jnp.where(bk == b, rb_ref[b, h], acc)

            o_ref[h] = lax.fori_loop(0, N_BUCKETS, step, jnp.zeros((SPAN, 2 * SPAN), F32))

    return pl.pallas_call(
        body, name="rel_bias_table",
        in_specs=[pl.BlockSpec(memory_space=pltpu.SMEM), pl.BlockSpec(memory_space=pltpu.VMEM)],
        out_specs=pl.BlockSpec(memory_space=pltpu.VMEM),
        out_shape=jax.ShapeDtypeStruct((N_HEADS, SPAN, 2 * SPAN), F32),
    )(rel_bias, buckets)


def _bias_grad_call(dbias, buckets):
    def body(db_ref, bk_ref, o_ref, rows_ref):
        for h in range(N_HEADS):
            bk = bk_ref[h // HEADS_PER_GROUP]
            dv = db_ref[h]

            def step(b, carry):
                rows_ref[h, b] = jnp.sum(jnp.where(bk == b, dv, 0.0), axis=0, keepdims=True)
                return carry

            lax.fori_loop(0, N_BUCKETS, step, 0)
        o_ref[...] = jnp.sum(rows_ref[...], axis=-1, keepdims=True)

    out = pl.pallas_call(
        body, name="rel_bias_grad",
        in_specs=[pl.BlockSpec(memory_space=pltpu.VMEM), pl.BlockSpec(memory_space=pltpu.VMEM)],
        out_specs=pl.BlockSpec(memory_space=pltpu.VMEM),
        out_shape=jax.ShapeDtypeStruct((N_HEADS, N_BUCKETS, 1, 1), F32),
        scratch_shapes=[pltpu.VMEM((N_HEADS, N_BUCKETS, 1, 2 * SPAN), F32)],
    )(dbias, buckets)
    return out.reshape(N_HEADS, N_BUCKETS).T


def _dot_nt(a, b):
    return lax.dot_general(a, b, (((1,), (1,)), ((), ())), preferred_element_type=F32)


def _dot_nn(a, b):
    return lax.dot_general(a, b, (((1,), (0,)), ((), ())), preferred_element_type=F32)


def _dot_tn(a, b):
    return lax.dot_general(a, b, (((0,), (0,)), ((), ())), preferred_element_type=F32)


def _band_masks(n, nb):
    qi = lax.broadcasted_iota(jnp.int32, (SPAN, SPAN), 0)
    ki = lax.broadcasted_iota(jnp.int32, (SPAN, SPAN), 1)
    prev_ok = jnp.logical_and(ki >= qi, n > 0)
    cur_ok = ki <= qi
    next_ok = jnp.logical_and(ki >= qi, n < nb - 1)
    return prev_ok, cur_ok, next_ok


def _wide_band_mask(n):
    qi = lax.broadcasted_iota(jnp.int32, (SPAN, 2 * SPAN), 0)
    ki = lax.broadcasted_iota(jnp.int32, (SPAN, 2 * SPAN), 1)
    prev_ok = jnp.logical_and(jnp.logical_and(ki < SPAN, ki >= qi), n > 0)
    cur_ok = jnp.logical_and(ki >= SPAN, ki - SPAN <= qi)
    return jnp.logical_or(prev_ok, cur_ok)


def _attn_plan(S, group):
    r = DILATED_PATTERNS[group][1]
    hp, per = (HEADS_PER_GROUP, 1) if r == 1 else (2, 4)
    return r, S // (r * SPAN), hp, per


def _residue_rows(rho, r):
    return slice(None) if r == 1 else pl.ds(rho, SPAN, stride=r)


def _for_residues(r, per, fn):
    if r == per:
        for u in range(per):
            fn(u)
        return

    def step(i, carry):
        for u in range(per):
            fn(i * per + u)
        return carry

    lax.fori_loop(0, r // per, step, 0)


def _attn_fwd_call(proj, bias, group, merge_with=None):
    S = proj.shape[0]
    r, nb, hp, per = _attn_plan(S, group)
    scale = HEAD_DIM ** -0.5
    kinds = ("q", "kp", "kc", "vp", "vc") if nb > 1 else ("q", "kc", "vc")
    merge = merge_with is not None
    assert not merge or (r == 1 and hp == HEADS_PER_GROUP and len(merge_with) == 4)

    per_kind = _refs_per_kind(r, hp)

    def body(*refs):
        ins = {kind: refs[i * per_kind:(i + 1) * per_kind] for i, kind in enumerate(kinds)}
        b_ref, *rest = refs[len(kinds) * per_kind:]
        if merge:
            o2_ref, s2_ref, o3_ref, s3_ref, a_ref, ab_ref, lse_ref = rest
        else:
            o_ref, lse_ref = rest
        n = pl.program_id(1)
        prev_ok, cur_ok, _ = _band_masks(n, nb)

        band_ok = _wide_band_mask(n) if nb > 1 else cur_ok

        def residue(rho):
            rows = _residue_rows(rho, r)
            for j in range(hp):
                get = lambda kind: _head_rows(ins[kind], j, rows, r).astype(BF16)
                q = get("q")
                if nb > 1:
                    keys, vals, bias_j = jnp.concatenate([get("kp"), get("kc")], axis=0), jnp.concatenate([get("vp"), get("vc")], axis=0), b_ref[j]
                else:
                    keys, vals, bias_j = get("kc"), get("vc"), b_ref[j, :, SPAN:]
                s = jnp.where(band_ok, _dot_nt(q, keys) * scale + bias_j, NEG_INF)
                m = jnp.max(s, axis=-1, keepdims=True)
                p = jnp.exp(s - m)
                den = jnp.sum(p, axis=-1, keepdims=True)
                o1 = _dot_nn(p.astype(BF16), vals) / den
                s1 = jnp.broadcast_to(m + jnp.log(den), (SPAN, HEAD_DIM))
                if not merge:
                    o_ref[j, rows, :] = o1
                    lse_ref[j, rows, :] = s1
                    continue
                sl = slice(j * HEAD_DIM, (j + 1) * HEAD_DIM)
                s2, s3 = s2_ref[j], s3_ref[j]
                mx = jnp.maximum(jnp.maximum(s1, s2), s3)
                w1 = jnp.exp(s1 - mx)
                w2 = jnp.exp(s2 - mx)
                w3 = jnp.exp(s3 - mx)
                total = w1 + w2 + w3
                merged = (w1 * o1 + w2 * o2_ref[j] + w3 * o3_ref[j]) / total
                a_ref[:, sl] = merged
                ab_ref[:, sl] = merged.astype(BF16)
                lse_ref[:, sl] = mx + jnp.log(total)

        _for_residues(r, per, residue)

    in_specs = [_head_spec(r, nb, hp, kind, group, jj) for kind in kinds for jj in range(per_kind)]
    in_specs.append(pl.BlockSpec((hp, SPAN, 2 * SPAN), lambda j, n: (group * (HEADS_PER_GROUP // hp) + j, 0, 0)))
    out = pl.BlockSpec((hp, r * SPAN, HEAD_DIM), lambda j, n: (j, n, 0))
    operands = [proj] * (len(in_specs) - 1) + [bias]
    if merge:
        in_specs += [out] * 4
        operands += list(merge_with)
        rows = pl.BlockSpec((SPAN, GROUP_WIDTH), lambda j, n: (n, 0))
        out_specs = [rows] * 3
        out_shape = [jax.ShapeDtypeStruct((S, GROUP_WIDTH), dt) for dt in (F32, BF16, F32)]
    else:
        out_specs = [out] * 2
        out_shape = [jax.ShapeDtypeStruct((HEADS_PER_GROUP, S, HEAD_DIM), F32)] * 2
    return pl.pallas_call(
        body, name=f"attn_fwd_g{group}", grid=(HEADS_PER_GROUP // hp, nb),
        in_specs=in_specs, out_specs=out_specs, out_shape=out_shape,
        compiler_params=_params(("parallel", "parallel"), VMEM_LIMIT),
    )(*operands)


_PROJ_PART = dict(q=0, qn=0, kp=1, kc=1, vp=2, vc=2)


def _refs_per_kind(r, hp):
    return 1 if r == 1 else hp


def _head_rows(refs, j, rows, r):
    return refs[0][:, j * HEAD_DIM:(j + 1) * HEAD_DIM] if r == 1 else refs[j][rows, :]


def _head_spec(r, nb, hp, kind, group, jj):
    if kind in _PROJ_PART:
        base = (_PROJ_PART[kind] * N_GROUPS + group) * HEADS_PER_GROUP
    else:
        base = 0
    if kind.endswith("p"):
        row = lambda n: jnp.maximum(n - 1, 0)
    elif kind.endswith("n"):
        row = lambda n: jnp.minimum(n + 1, nb - 1)
    else:
        row = lambda n: n
    if r == 1:
        return pl.BlockSpec((SPAN, hp * HEAD_DIM), lambda j, n: (row(n), base // hp + j))
    return pl.BlockSpec((r * SPAN, HEAD_DIM), lambda j, n: (row(n), base + j * hp + jj))


def _attn_bwd_call(proj, bias, a, da, lse, group, dproj):
    S = proj.shape[0]
    r, nb, hp, per = _attn_plan(S, group)
    scale = HEAD_DIM ** -0.5
    kinds = ("q", "qn", "kp", "kc", "vp", "vc", "da", "dan", "lse", "lsen", "a", "an") if nb > 1 else ("q", "kc", "vc", "da", "lse", "a")
    source = dict(da=da, dan=da, lse=lse, lsen=lse, a=a, an=a)

    per_kind = _refs_per_kind(r, hp)
    per_group = HEADS_PER_GROUP // hp
    block_rows = r * SPAN

    def body(*refs):
        ins = {kind: refs[i * per_kind:(i + 1) * per_kind] for i, kind in enumerate(kinds)}
        b_ref, _, dproj_ref, db_ref, stage_ref, sems = refs[len(kinds) * per_kind:][:6]
        strided_ref = None if r == 1 else refs[-1]
        jg, n = pl.program_id(0), pl.program_id(1)
        prev_ok, cur_ok, next_ok = _band_masks(n, nb)

        @pl.when(n == 0)
        def _():
            db_ref[...] = jnp.zeros_like(db_ref)

        band_ok = _wide_band_mask(n) if nb > 1 else cur_ok

        def fill(slot):
            def put(part, j, rows, value):
                if r == 1:
                    stage_ref[slot, part, :, j * HEAD_DIM:(j + 1) * HEAD_DIM] = value.astype(BF16)
                else:
                    strided_ref[part, j, rows, :] = value

            _for_residues(r, per, functools.partial(residue, put))
            if r > 1:
                for part in range(3):
                    for j in range(hp):
                        for t0 in range(0, block_rows, ROW_TILE):
                            stage_ref[slot, part, t0:t0 + ROW_TILE, j * HEAD_DIM:(j + 1) * HEAD_DIM] = (
                                strided_ref[part, j, t0:t0 + ROW_TILE, :].astype(BF16))

        def residue(put, rho):
            rows = _residue_rows(rho, r)
            for j in range(hp):
                get = lambda kind: _head_rows(ins[kind], j, rows, r)
                q = get("q").astype(BF16)
                kc = get("kc").astype(BF16)
                vc = get("vc").astype(BF16)
                da_q = get("da")
                dav = da_q.astype(BF16)
                lse_q = get("lse")
                dl_q = jnp.sum(get("a") * da_q, axis=-1, keepdims=True)
                if nb == 1:
                    pc = jnp.exp(jnp.where(cur_ok, _dot_nt(q, kc) * scale + b_ref[j, :, SPAN:], NEG_INF) - lse_q)
                    dsc = pc * (_dot_nt(dav, vc) - dl_q)
                    dsc_b = dsc.astype(BF16)
                    dq = _dot_nn(dsc_b, kc)
                    dk = _dot_tn(dsc_b, q)
                    dv = _dot_tn(pc.astype(BF16), dav)
                    db_ref[j, :, SPAN:] += dsc
                else:
                    qn = get("qn").astype(BF16)
                    da_n = get("dan")
                    dan = da_n.astype(BF16)
                    keys = jnp.concatenate([get("kp").astype(BF16), kc], axis=0)
                    vals = jnp.concatenate([get("vp").astype(BF16), vc], axis=0)
                    wide = lambda t: jnp.concatenate([t, t], axis=1)
                    p = jnp.exp(jnp.where(band_ok, _dot_nt(q, keys) * scale + b_ref[j], NEG_INF) - wide(lse_q))
                    ds = p * (_dot_nt(dav, vals) - dl_q)
                    dq = _dot_nn(ds.astype(BF16), keys)
                    db_ref[j] += ds
                    pn = jnp.exp(jnp.where(next_ok, _dot_nt(qn, kc) * scale + b_ref[j, :, :SPAN], NEG_INF) - get("lsen"))
                    dsn = pn * (_dot_nt(dan, vc) - jnp.sum(get("an") * da_n, axis=-1, keepdims=True))
                    both = lambda cur_part, next_part: jnp.concatenate([cur_part.astype(BF16), next_part.astype(BF16)], axis=0)
                    dk = _dot_tn(both(ds[:, SPAN:], dsn), jnp.concatenate([q, qn], axis=0))
                    dv = _dot_tn(both(p[:, SPAN:], pn), jnp.concatenate([dav, dan], axis=0))
                put(0, j, rows, dq * scale)
                put(1, j, rows, dk * scale)
                put(2, j, rows, dv)

        cols = [(part * N_GROUPS + group) * GROUP_WIDTH + jg * (hp * HEAD_DIM) for part in range(3)]
        rows = pl.ds(pl.multiple_of(n * block_rows, SPAN), block_rows)
        _staged_window_stores(stage_ref, dproj_ref, sems, jg * nb + n, per_group * nb, rows, cols, fill)

    band = (hp, SPAN, 2 * SPAN)
    in_specs = [_head_spec(r, nb, hp, kind, group, jj) for kind in kinds for jj in range(per_kind)]
    in_specs += [pl.BlockSpec(band, lambda j, n: (group * per_group + j, 0, 0)), ANY]
    operands = [source.get(kind, proj) for kind in kinds for _ in range(per_kind)] + [bias, dproj]
    scratch = [pltpu.VMEM((2, 3, block_rows, hp * HEAD_DIM), BF16), pltpu.SemaphoreType.DMA((2, 3))]
    if r > 1:
        scratch.append(pltpu.VMEM((3, hp, block_rows, HEAD_DIM), F32))
    return pl.pallas_call(
        body, name=f"attn_bwd_g{group}", grid=(per_group, nb),
        in_specs=in_specs,
        out_specs=[ANY, pl.BlockSpec(band, lambda j, n: (j, 0, 0))],
        out_shape=[jax.ShapeDtypeStruct(dproj.shape, BF16), jax.ShapeDtypeStruct((HEADS_PER_GROUP, SPAN, 2 * SPAN), F32)],
        input_output_aliases={len(operands) - 1: 0},
        scratch_shapes=scratch,
        compiler_params=_params(("arbitrary", "arbitrary"), VMEM_LIMIT),
    )(*operands)


def _tap_rows(xpad_ref, t0, k, width, pad):
    return xpad_ref[pl.ds(t0 + (pad - (width - 1 - k)), TIME_BLOCK), :]


def _conv_block(xpad_ref, t0, w_ref, width, pad):
    acc = None
    for k in range(width):
        term = w_ref[k:k + 1, :] * _tap_rows(xpad_ref, t0, k, width, pad)
        acc = term if acc is None else acc + term
    return acc


def _conv_transpose_block(dpad_ref, t0, w_ref, width):
    acc = None
    for k in range(width):
        term = w_ref[k:k + 1, :] * dpad_ref[pl.ds(t0 + (width - 1 - k), TIME_BLOCK), :]
        acc = term if acc is None else acc + term
    return acc


def _conv_weight_grad(xpad_ref, t0, dy, dw_ref, width, pad):
    for k in range(width):
        dw_ref[k:k + 1, :] += jnp.sum(dy * _tap_rows(xpad_ref, t0, k, width, pad), axis=0, keepdims=True)


def _time_loop(S, step, skip_first=0, skip_last=0):
    def it(tb, carry):
        step(pl.multiple_of(tb * TIME_BLOCK, TIME_BLOCK))
        return carry

    lax.fori_loop(skip_first, S // TIME_BLOCK - skip_last, it, 0)


def _fill_head(head_ref, x_ref, pad):
    head_ref[0:pad, :] = jnp.zeros((pad, LANES), F32)
    head_ref[pad:, :] = x_ref[0:TIME_BLOCK, :]


def _fill_tail(tail_ref, x_ref, pad):
    S = x_ref.shape[0]
    tail_ref[0:TIME_BLOCK, :] = x_ref[S - TIME_BLOCK:S, :]
    tail_ref[TIME_BLOCK:, :] = jnp.zeros((pad, LANES), F32)


def _conv_fwd_call(proj, col0, w, b):
    S = proj.shape[0]
    C = w.shape[1]
    nt = C // LANES
    v0, g0 = col0 // LANES, (col0 + C) // LANES

    def body(val_ref, gate_ref, w_ref, b_ref, o_ref, pad_ref):
        pad_ref[0:CONV_PAD, :] = jnp.zeros((CONV_PAD, LANES), F32)
        pad_ref[CONV_PAD:, :] = val_ref[...] * _sigmoid(gate_ref[...])

        def step(t0):
            o_ref[pl.ds(t0, TIME_BLOCK), :] = _conv_block(pad_ref, t0, w_ref, CONV_WIDTH, CONV_PAD) + b_ref[...]

        _time_loop(S, step)

    seq = lambda off: pl.BlockSpec((S, LANES), lambda i: (0, off + i))
    return pl.pallas_call(
        body, name="conv_module", grid=(nt,),
        in_specs=[seq(v0), seq(g0), pl.BlockSpec((CONV_WIDTH, LANES), lambda i: (0, i)), pl.BlockSpec((1, LANES), lambda i: (0, i))],
        out_specs=seq(0), out_shape=jax.ShapeDtypeStruct((S, C), F32),
        scratch_shapes=[pltpu.VMEM((S + CONV_PAD, LANES), F32)],
        compiler_params=_params(("parallel",)),
    )(proj, proj, w, b)


def _conv_bwd_call(proj, col0, w, dc1, dproj):
    S = proj.shape[0]
    C = w.shape[1]
    nt = C // LANES
    v0, g0 = col0 // LANES, (col0 + C) // LANES

    def body(val_ref, gate_ref, w_ref, dy_ref, _, dproj_ref, dw_ref, db_ref, xpad_ref, tail_ref, dwacc_ref, stage_ref, sems):
        i = pl.program_id(0)
        xpad_ref[0:CONV_PAD, :] = jnp.zeros((CONV_PAD, LANES), F32)
        xpad_ref[CONV_PAD:, :] = val_ref[...] * _sigmoid(gate_ref[...])
        _fill_tail(tail_ref, dy_ref, CONV_PAD)
        dwacc_ref[...] = jnp.zeros_like(dwacc_ref)

        def fill(slot):
            def block(t0, dy_src, dy_t0):
                rows = pl.ds(t0, TIME_BLOCK)
                _conv_weight_grad(xpad_ref, t0, dy_ref[rows, :], dwacc_ref, CONV_WIDTH, CONV_PAD)
                dc0 = _conv_transpose_block(dy_src, dy_t0, w_ref, CONV_WIDTH)
                sg = _sigmoid(gate_ref[rows, :])
                stage_ref[slot, 0, rows, :] = (dc0 * sg).astype(BF16)
                stage_ref[slot, 1, rows, :] = (dc0 * val_ref[rows, :] * sg * (1.0 - sg)).astype(BF16)

            _time_loop(S, lambda t0: block(t0, dy_ref, t0), skip_last=1)
            block(S - TIME_BLOCK, tail_ref, 0)

        _staged_window_stores(stage_ref, dproj_ref, sems, i, nt, pl.ds(0, S),
                              [col0 + i * LANES, col0 + C + i * LANES], fill)
        dw_ref[...] = dwacc_ref[...]
        db_ref[...] = jnp.sum(dy_ref[...], axis=0, keepdims=True)

    seq = lambda off: pl.BlockSpec((S, LANES), lambda i: (0, off + i))
    return pl.pallas_call(
        body, name="conv_module_bwd", grid=(nt,),
        in_specs=[seq(v0), seq(g0), pl.BlockSpec((CONV_WIDTH, LANES), lambda i: (0, i)), seq(0), ANY],
        out_specs=[ANY, pl.BlockSpec((CONV_PAD, LANES), lambda i: (0, i)), pl.BlockSpec((1, LANES), lambda i: (0, i))],
        out_shape=[jax.ShapeDtypeStruct(dproj.shape, BF16),
                   jax.ShapeDtypeStruct((CONV_PAD, C), F32), jax.ShapeDtypeStruct((1, C), F32)],
        input_output_aliases={4: 0},
        scratch_shapes=[pltpu.VMEM((S + CONV_PAD, LANES), F32), pltpu.VMEM((TIME_BLOCK + CONV_PAD, LANES), F32),
                        pltpu.VMEM((CONV_PAD, LANES), F32),
                        pltpu.VMEM((2, 2, S, LANES), BF16), pltpu.SemaphoreType.DMA((2, 2))],
        compiler_params=_params(("arbitrary",)),
    )(proj, proj, w, dc1, dproj)


def _ffn_fwd_call(u, w, b):
    S, C2 = u.shape
    C = C2 // 2
    width = _pick(C, (2 * LANES, LANES))
    nt = C // width

    def body(ug_ref, uv_ref, wg_ref, wv_ref, bg_ref, bv_ref, f_ref, xg_ref, xv_ref):
        zeros = jnp.zeros((FFN_PAD, LANES), F32)
        for part in range(width // LANES):
            lanes = slice(part * LANES, (part + 1) * LANES)
            xg_ref[0:FFN_PAD, :] = zeros
            xv_ref[0:FFN_PAD, :] = zeros
            xg_ref[FFN_PAD:, :] = ug_ref[:, lanes]
            xv_ref[FFN_PAD:, :] = uv_ref[:, lanes]

            def step(t0, lanes=lanes):
                cg = _conv_block(xg_ref, t0, wg_ref.at[:, lanes], FFN_CONV_WIDTH, FFN_PAD) + bg_ref[:, lanes]
                cv = _conv_block(xv_ref, t0, wv_ref.at[:, lanes], FFN_CONV_WIDTH, FFN_PAD) + bv_ref[:, lanes]
                f_ref[pl.ds(t0, TIME_BLOCK), lanes] = (_gelu(cg) * cv).astype(BF16)

            _time_loop(S, step)

    seq = lambda off: pl.BlockSpec((S, width), lambda i: (0, off + i))
    wsp = lambda off: pl.BlockSpec((FFN_CONV_WIDTH, width), lambda i: (0, off + i))
    bsp = lambda off: pl.BlockSpec((1, width), lambda i: (0, off + i))
    return pl.pallas_call(
        body, name="ffn_conv_geglu", grid=(nt,),
        in_specs=[seq(0), seq(nt), wsp(0), wsp(nt), bsp(0), bsp(nt)],
        out_specs=seq(0), out_shape=jax.ShapeDtypeStruct((S, C), BF16),
        scratch_shapes=[pltpu.VMEM((FFN_PAD + S, LANES), F32)] * 2,
        compiler_params=_params(("parallel",)),
    )(u, u, w, w, b, b)


def _ffn_bwd_call(u, w, b, df):
    S, C2 = u.shape
    C = C2 // 2
    nt = C // LANES

    def body(ug_ref, uv_ref, wg_ref, wv_ref, bg_ref, bv_ref, df_ref,
             du_ref, dwg_ref, dwv_ref, dbg_ref, dbv_ref,
             hg_ref, hv_ref, dg_ref, dv_ref, dwg_acc, dwv_acc, dbg_acc, dbv_acc):
        zeros = jnp.zeros((FFN_PAD, LANES), F32)
        _fill_head(hg_ref, ug_ref, FFN_PAD)
        _fill_head(hv_ref, uv_ref, FFN_PAD)
        dg_ref[S:, :] = zeros
        dv_ref[S:, :] = zeros
        dwg_acc[...] = jnp.zeros_like(dwg_acc)
        dwv_acc[...] = jnp.zeros_like(dwv_acc)
        dbg_acc[...] = jnp.zeros_like(dbg_acc)
        dbv_acc[...] = jnp.zeros_like(dbv_acc)

        def first(t0, xg_ref, xv_ref, x_t0, pad):
            rows = pl.ds(t0, TIME_BLOCK)
            cg = _conv_block(xg_ref, x_t0, wg_ref, FFN_CONV_WIDTH, pad) + bg_ref[...]
            cv = _conv_block(xv_ref, x_t0, wv_ref, FFN_CONV_WIDTH, pad) + bv_ref[...]
            dfb = df_ref[rows, :]
            gelu, gelu_grad = _gelu_and_grad(cg)
            dcg = dfb * cv * gelu_grad
            dcv = dfb * gelu
            dg_ref[rows, :] = dcg
            dv_ref[rows, :] = dcv
            _conv_weight_grad(xg_ref, x_t0, dcg, dwg_acc, FFN_CONV_WIDTH, pad)
            _conv_weight_grad(xv_ref, x_t0, dcv, dwv_acc, FFN_CONV_WIDTH, pad)
            dbg_acc[...] += jnp.sum(dcg, axis=0, keepdims=True)
            dbv_acc[...] += jnp.sum(dcv, axis=0, keepdims=True)

        def second(t0):
            rows = pl.ds(t0, TIME_BLOCK)
            du_ref[0, rows, :] = _conv_transpose_block(dg_ref, t0, wg_ref, FFN_CONV_WIDTH).astype(BF16)
            du_ref[1, rows, :] = _conv_transpose_block(dv_ref, t0, wv_ref, FFN_CONV_WIDTH).astype(BF16)

        first(0, hg_ref, hv_ref, 0, FFN_PAD)
        _time_loop(S, lambda t0: first(t0, ug_ref, uv_ref, t0, 0), skip_first=1)
        _time_loop(S, second)
        dwg_ref[...] = dwg_acc[...]
        dwv_ref[...] = dwv_acc[...]
        dbg_ref[...] = dbg_acc[...]
        dbv_ref[...] = dbv_acc[...]

    seq = lambda off: pl.BlockSpec((S, LANES), lambda i: (0, off + i))
    wsp = lambda off: pl.BlockSpec((FFN_CONV_WIDTH, LANES), lambda i: (0, off + i))
    bsp = lambda off: pl.BlockSpec((1, LANES), lambda i: (0, off + i))
    return pl.pallas_call(
        body, name="ffn_conv_geglu_bwd", grid=(nt,),
        in_specs=[seq(0), seq(nt), wsp(0), wsp(nt), bsp(0), bsp(nt), seq(0)],
        out_specs=[pl.BlockSpec((2, S, LANES), lambda i: (0, 0, i)),
                   pl.BlockSpec((SUBLANES, LANES), lambda i: (0, i)), pl.BlockSpec((SUBLANES, LANES), lambda i: (0, i)),
                   bsp(0), bsp(0)],
        out_shape=[jax.ShapeDtypeStruct((2, S, C), BF16)] + [jax.ShapeDtypeStruct((SUBLANES, C), F32)] * 2
        + [jax.ShapeDtypeStruct((1, C), F32)] * 2,
        scratch_shapes=[pltpu.VMEM((FFN_PAD + TIME_BLOCK, LANES), F32)] * 2 + [pltpu.VMEM((S + FFN_PAD, LANES), F32)] * 2
        + [pltpu.VMEM((SUBLANES, LANES), F32)] * 2
        + [pltpu.VMEM((1, LANES), F32)] * 2,
        compiler_params=_params(("parallel",)),
    )(u, u, w, w, b, b, df)


def _adamw(w_ref, g_ref, m_ref, v_ref, d_ref, mo_ref, vo_ref):
    gv = g_ref[...]
    mn = ADAM_B1 * m_ref[...] + (1.0 - ADAM_B1) * gv
    vn = ADAM_B2 * v_ref[...] + (1.0 - ADAM_B2) * (gv * gv)
    mo_ref[...] = mn
    vo_ref[...] = vn
    m_hat = mn * (1.0 / (1.0 - ADAM_B1 ** ADAM_STEP))
    v_hat = vn * (1.0 / (1.0 - ADAM_B2 ** ADAM_STEP))
    d_ref[...] = -ADAM_LR * (m_hat / (jnp.sqrt(v_hat) + ADAM_EPS) + ADAM_WD * w_ref[...])


def _adamw_call(w, g, m, v, name):
    R, C = w.shape
    tr = _row_tile(R, C)

    def body(w_ref, g_ref, m_ref, v_ref, go_ref, d_ref, mo_ref, vo_ref):
        go_ref[...] = g_ref[...]
        _adamw(w_ref, g_ref, m_ref, v_ref, d_ref, mo_ref, vo_ref)

    spec = pl.BlockSpec((tr, C), lambda i: (i, 0))
    return pl.pallas_call(
        body, name=name, grid=(R // tr,),
        in_specs=[spec] * 4, out_specs=[spec] * 4,
        out_shape=[jax.ShapeDtypeStruct((R, C), F32)] * 4,
        compiler_params=_params(("parallel",)),
    )(w, g, m, v)


def _adamw_small_call(ws, gs, ms, vs):
    n = len(ws)

    def body(*refs):
        w_refs, g_refs, m_refs, v_refs, d_refs, mo_refs, vo_refs = (refs[i * n:(i + 1) * n] for i in range(7))
        for i in range(n):
            _adamw(w_refs[i], g_refs[i], m_refs[i], v_refs[i], d_refs[i], mo_refs[i], vo_refs[i])

    whole = pl.BlockSpec(memory_space=pltpu.VMEM)
    outs = pl.pallas_call(
        body, name="adamw_small",
        in_specs=[whole] * (4 * n), out_specs=[whole] * (3 * n),
        out_shape=[jax.ShapeDtypeStruct(w.shape, F32) for w in ws] * 3,
    )(*ws, *gs, *ms, *vs)
    return outs[:n], outs[n:2 * n], outs[2 * n:]


def _position():
    return lax.axis_index("x"), lax.axis_index("y"), lax.axis_index("c")


def _chip_peers(x, y):
    return [(x, 1 - y), (1 - x, y), (1 - x, 1 - y)]


def _half_rows(ref, core, rows):
    h = rows // 2
    start = pl.multiple_of(core * h, PACKED_ROWS)
    return ref.at[pl.ds(start, h), :] if len(ref.shape) == 2 else ref.at[:, pl.ds(start, h), :]


def _shard_half(ref, shard, core, rows):
    h = rows // 2
    return ref.at[shard, pl.ds(pl.multiple_of(core * h, PACKED_ROWS), h), :]


ANY = pl.BlockSpec(memory_space=pl.ANY)


def _first_hop_copies(srcs, lands):
    x, y, c = _position()
    chip = 2 * x + y
    targets = [(px, py, c) for px, py in _chip_peers(x, y)] + [(x, y, 1 - c)]
    rows = srcs[0].shape[0]
    out = []
    for i, (s, l) in enumerate(zip(srcs, lands)):
        for k, dev in enumerate(targets):
            if i == 0 and k < 3:
                out.append((_half_rows(s, c, rows), _shard_half(l, chip, c, rows), dev, k))
            else:
                out.append((s, l.at[chip], dev, len(targets) * i + k))
    return out


def _second_hop_copies(srcs, lands):
    x, y, c = _position()
    rows = lands[0].shape[1]
    out = []
    for k, (px, py) in enumerate(_chip_peers(x, y)):
        half = _shard_half(lands[0], 2 * px + py, c, rows)
        out.append((half, half, (x, y, 1 - c), k))
    return out


HBM_SPEC = pl.BlockSpec(memory_space=pltpu.HBM)
SEM_SPEC = pl.BlockSpec(memory_space=pltpu.SEMAPHORE)
DATAFLOW = pltpu.SideEffectType.DATAFLOW_SIDE_EFFECTING


def _in_hbm(a):
    return pltpu.with_memory_space_constraint(a, pltpu.HBM)


SIBLING_HANDSHAKE_IDS = dict(grad_exchange_start_w_up=1, grad_assemble_start_others=2, grad_assemble_start_w_in=3)


def _split_start(name, groups, after, carry=None, sibling_only=False):
    spans, arrays = [], []
    for srcs, lands, _, _ in groups:
        spans.append((len(arrays), len(srcs), len(lands)))
        arrays += list(srcs) + list(lands)
    if carry is not None:
        arrays.append(carry)
    na, ng = len(arrays), len(groups)

    def body(*refs):
        sems, token = refs[na + 1:na + 1 + 2 * ng], refs[-1]
        if sibling_only:
            x, y, c = _position()
            barrier = pltpu.get_barrier_semaphore()
            pl.semaphore_signal(barrier, inc=1, device_id=(x, y, 1 - c), device_id_type=MESH)
            pl.semaphore_wait(barrier, 1)
        for g, (_, _, _, copies) in enumerate(groups):
            off, ns, nl = spans[g]
            for src, dst, dev, idx in copies(refs[off:off + ns], refs[off + ns:off + ns + nl]):
                pltpu.make_async_remote_copy(src_ref=src, dst_ref=dst, send_sem=sems[2 * g].at[idx], recv_sem=sems[2 * g + 1].at[idx],
                                             device_id=dev, device_id_type=MESH).start()
        token[...] = jnp.zeros_like(token)

    outs = pl.pallas_call(
        body, name=name,
        in_specs=[HBM_SPEC] * na + [ANY],
        out_specs=[SEM_SPEC] * (2 * ng) + [HBM_SPEC] * na + [pl.BlockSpec(memory_space=pltpu.VMEM)],
        out_shape=[pltpu.SemaphoreType.DMA((n_sems,)) for _, _, n_sems, _ in groups for _ in range(2)]
        + [pltpu.HBM(a.shape, a.dtype) for a in arrays] + [jax.ShapeDtypeStruct((SUBLANES, LANES), F32)],
        input_output_aliases={i: 2 * ng + i for i in range(na)},
        compiler_params=pltpu.CompilerParams(has_side_effects=DATAFLOW,
                                             collective_id=SIBLING_HANDSHAKE_IDS[name] if sibling_only else None),
    )(*[_in_hbm(a) for a in arrays], after)
    started = []
    for g, (off, ns, nl) in enumerate(spans):
        thru = outs[2 * ng + off:2 * ng + off + ns + nl]
        started.append(dict(send=outs[2 * g], recv=outs[2 * g + 1], srcs=list(thru[:ns]), lands=list(thru[ns:]),
                            tile=outs[-1], token=outs[-1][0, 0], carry=None if carry is None else outs[2 * ng + na - 1]))
    return started


def _split_wait(name, started, copies, after):
    n, m = len(started["srcs"]), len(started["lands"])
    after = list(after) if isinstance(after, (list, tuple)) else [after]

    def body(*refs):
        src_refs, land_refs = refs[:n], refs[n:n + m]
        send_sem, recv_sem = refs[n + m], refs[n + m + 1]
        for src, dst, dev, idx in copies(src_refs, land_refs):
            cp = pltpu.make_async_remote_copy(src_ref=src, dst_ref=dst, send_sem=send_sem.at[idx], recv_sem=recv_sem.at[idx],
                                              device_id=dev, device_id_type=MESH)
            cp.wait_send()
            cp.wait_recv()

    arrays = started["srcs"] + started["lands"]
    outs = pl.pallas_call(
        body, name=name,
        in_specs=[HBM_SPEC] * (n + m) + [SEM_SPEC, SEM_SPEC] + [ANY] * len(after),
        out_specs=[HBM_SPEC] * (n + m),
        out_shape=[pltpu.HBM(a.shape, a.dtype) for a in arrays],
        input_output_aliases={i: i for i in range(n + m)},
        compiler_params=pltpu.CompilerParams(has_side_effects=DATAFLOW),
    )(*arrays, started["send"], started["recv"], *after)
    return list(outs)


def _gather_copies(srcs, lands):
    x, y, c = _position()
    chip = 2 * x + y
    targets = [(px, py, c) for px, py in _chip_peers(x, y)] + [(x, y, 1 - c)]
    return [(s, l.at[chip], dev, len(targets) * i + k) for i, (s, l) in enumerate(zip(srcs, lands)) for k, dev in enumerate(targets)]


def _sibling_copies(srcs, lands):
    x, y, c = _position()
    return [(_half_rows(srcs[0], 1 - c, srcs[0].shape[1]), lands[0], (x, y, 1 - c), 0)]


def _sibling_whole_copies(srcs, lands):
    x, y, c = _position()
    return [(srcs[0], lands[0], (x, y, 1 - c), 0)]


def _exchange_copies(srcs, lands):
    x, y, c = _position()
    return [(srcs[0].at[2 * px + py], lands[0].at[k], (px, py, c), k) for k, (px, py) in enumerate(_chip_peers(x, y))]


def _pair_sum_call(grad, recv, chip_core, name):
    _, h, B = recv.shape
    tr = _row_tile(h, B)

    def body(cc_ref, g_ref, r_ref, o_ref, ob_ref):
        s = g_ref[...] + r_ref[...]
        ob_ref[...] = s.astype(BF16)

        @pl.when(pl.program_id(1) == cc_ref[0])
        def _():
            o_ref[...] = s

    g_spec = pl.BlockSpec((None, tr, B), lambda i, q, cc_ref: (q, cc_ref[1] * (h // tr) + i, 0))
    spec = pl.BlockSpec((None, tr, B), lambda i, q, cc_ref: (q, i, 0))
    own_spec = pl.BlockSpec((tr, B), lambda i, q, cc_ref: (i, 0))
    return pl.pallas_call(
        body, name=name,
        grid_spec=pltpu.PrefetchScalarGridSpec(num_scalar_prefetch=1, grid=(h // tr, N_CHIPS), in_specs=[g_spec, spec],
                                               out_specs=[own_spec, spec]),
        out_shape=[jax.ShapeDtypeStruct((h, B), F32), jax.ShapeDtypeStruct(recv.shape, BF16)],
        compiler_params=_params(("parallel", "arbitrary")),
    )(chip_core, grad, recv)


def _chip_sum_call(partial, recv, chip_core, name):
    _, h, B = recv.shape
    tr = _row_tile(h, B)

    def body(cc_ref, p_ref, r_ref, o_ref):
        o_ref[...] = ((p_ref[...] + r_ref[0].astype(F32)) + r_ref[1].astype(F32)) + r_ref[2].astype(F32)

    return pl.pallas_call(
        body, name=name,
        grid_spec=pltpu.PrefetchScalarGridSpec(
            num_scalar_prefetch=1, grid=(h // tr,),
            in_specs=[pl.BlockSpec((tr, B), lambda i, cc_ref: (i, 0)),
                      pl.BlockSpec((3, tr, B), lambda i, cc_ref: (0, i, 0))],
            out_specs=pl.BlockSpec((tr, B), lambda i, cc_ref: (cc_ref[1] * (h // tr) + i, 0))),
        out_shape=jax.ShapeDtypeStruct((2 * h, B), F32),
        compiler_params=_params(("parallel",)),
    )(chip_core, partial, recv)


def _assemble_copies(srcs, lands):
    x, y, c = _position()
    out = []
    for i, land in enumerate(lands):
        half = _half_rows(land, c, land.shape[0])
        out.append((half, half, (x, y, 1 - c), i))
    return out


N_DEVICES = 8


def _allsum_copies(srcs, lands):
    x, y, c = _position()
    me = 4 * x + 2 * y + c
    out = []
    for k in range(1, N_DEVICES):
        peer = (1 - x if k & 4 else x, 1 - y if k & 2 else y, 1 - c if k & 1 else c)
        out.append((srcs[0], lands[0].at[me], peer, k - 1))
    return out


def _ordered_sum_call(mine, landed, me_chip, shapes, sharded_cols):
    rows = mine.shape[0]
    outs = [(s[0], n) if n else s for s, n in zip(shapes, sharded_cols)]

    def body(mc_ref, x_ref, l_ref, *refs):
        acc_ref = refs[-1]
        acc = jnp.where(mc_ref[0] == 0, x_ref[...], l_ref[0])
        for d in range(1, N_DEVICES):
            acc = acc + jnp.where(mc_ref[0] == d, x_ref[...], l_ref[d])
        acc_ref[...] = acc
        first = 0
        for o_ref, (r, c), n in zip(refs[:-1], shapes, sharded_cols):
            per_row = c // LANES

            def unpack(chip, o_ref=o_ref, r=r, n=n, per_row=per_row, first=first):
                for i in range(r):
                    for j in range((n or per_row * LANES) // LANES):
                        src = first + i * per_row + chip * ((n or 0) // LANES) + j
                        o_ref[i:i + 1, j * LANES:(j + 1) * LANES] = acc_ref[src:src + 1, :]

            if n:
                for q in range(N_CHIPS):
                    pl.when(mc_ref[1] == q)(functools.partial(unpack, q))
            else:
                unpack(0)
            first += r * per_row

    results = pl.pallas_call(
        body, name="small_grad_sum",
        in_specs=[pl.BlockSpec(memory_space=pltpu.SMEM), pl.BlockSpec(memory_space=pltpu.VMEM), pl.BlockSpec(memory_space=pltpu.VMEM)],
        out_specs=[pl.BlockSpec(memory_space=pltpu.VMEM)] * len(outs),
        out_shape=[jax.ShapeDtypeStruct(s, F32) for s in outs],
        scratch_shapes=[pltpu.VMEM((rows, LANES), F32)],
    )(me_chip, mine, landed)
    return results


def _pack(arrays):
    flat = jnp.concatenate([a.reshape(-1).astype(F32) for a in arrays])
    rows = -(-flat.shape[0] // LANES)
    rows = -(-rows // SUBLANES) * SUBLANES
    flat = jnp.pad(flat, (0, rows * LANES - flat.shape[0]))
    return flat.reshape(rows, LANES)


def _local_step(xs, target, P, late_weights, on_grad):
    S, D = xs.shape
    qkv_width = 3 * N_HEADS * HEAD_DIM
    glu_col0, gate_col0 = qkv_width, qkv_width + 2 * D
    shard_major = lambda g: g.reshape(N_CHIPS, g.shape[0] // N_CHIPS, g.shape[1])

    h1 = _rms_fwd_call(xs, P["norm_mix_pre"])
    buckets = _bucket_tables()
    bias = _bias_table_call(P["rel_bias"] + 0.0 * h1[0, 0].astype(F32), buckets)
    P = dict(P, **late_weights("in", bias))
    proj = _matmul(h1, P["w_in"], "nn", "proj_in")
    dilated = []
    for g in range(1, N_GROUPS):
        dilated += _attn_fwd_call(proj, bias, g)
    a, a_bf, lse = _attn_fwd_call(proj, bias, 0, merge_with=dilated)
    P = dict(P, **late_weights("mix", a_bf))
    y_a = _matmul(a_bf, P["w_attn_out"], "nn", "attn_out")
    c1 = _conv_fwd_call(proj, glu_col0, P["conv_dw_w"], P["conv_dw_b"])
    cact = _ln_silu_call(c1, P["conv_ln_g"], P["conv_ln_b"])
    y_c = _matmul(cact, P["conv_pw_w"], "nn", "conv_pw")
    mixed = _mix_call(proj, gate_col0, P["b_gate"], y_a, y_c)
    out = _matmul(mixed, P["w_out"], "nn", "mix_out")
    x1, h2 = _res1_call(xs, out, P["norm_mix_post"], P["norm_ffn_pre"])
    P = dict(P, **late_weights("up", h2))
    u = _matmul(h2, P["w_up"], "nn", "ffn_up")
    f = _ffn_fwd_call(u, P["ffn_conv_w"], P["ffn_conv_b"])
    P = dict(P, **late_weights("down", f))
    loss_tile, dx2, dyff, dg_ffn_post = _ffn_down_loss_call(f, P["w_down"], x1, P["norm_ffn_post"], target)

    G = {}
    G["norm_ffn_post"] = dg_ffn_post
    on_grad("w_down", shard_major(_matmul(f, dyff, "tn", "ffn_down_dw")))
    df = _matmul(dyff, P["w_down"], "nt", "ffn_down_dx")
    du, dwg, dwv, dbg, dbv = _ffn_bwd_call(u, P["ffn_conv_w"], P["ffn_conv_b"], df)
    G["ffn_conv_w"] = jnp.concatenate([dwg[:FFN_CONV_WIDTH], dwv[:FFN_CONV_WIDTH]], axis=1)
    G["ffn_conv_b"] = jnp.concatenate([dbg, dbv], axis=1)
    du = on_grad("w_up", functools.partial(_grad_half_matmul, h2, "ffn_up_dw"), carry=du)
    dh2 = _matmul(du, P["w_up"], "nt", "ffn_up_dx")
    dx1, dout, G["norm_ffn_pre"], G["norm_mix_post"] = _mid_bwd_call(x1, P["norm_ffn_pre"], dh2, dx2, out, P["norm_mix_post"])
    on_grad("w_out", shard_major(_matmul(mixed, dout, "tn", "mix_out_dw")))
    dmixed = _matmul(dout, P["w_out"], "nt", "mix_out_dx")
    dya, dyc, dproj, dba, dbc = _mix_bwd_call(dmixed, proj, gate_col0, P["b_gate"], y_a, y_c)
    G["b_gate"] = jnp.concatenate([dba, dbc], axis=1)
    on_grad("w_attn_out", _matmul(a_bf, dya, "tn", "attn_out_dw", out_shards=True))
    dyc = on_grad("conv_pw_w", shard_major(_matmul(cact, dyc, "tn", "conv_pw_dw")), carry=dyc)
    da = _matmul(dya, P["w_attn_out"], "nt", "attn_out_dx")
    dcact = _matmul(dyc, P["conv_pw_w"], "nt", "conv_pw_dx")
    dc1, G["conv_ln_g"], G["conv_ln_b"] = _ln_silu_bwd_call(c1, P["conv_ln_g"], P["conv_ln_b"], dcact)
    dproj, dw_dw, G["conv_dw_b"] = _conv_bwd_call(proj, glu_col0, P["conv_dw_w"], dc1, dproj)
    G["conv_dw_w"] = dw_dw[:CONV_WIDTH]
    dbs = []
    for g in range(N_GROUPS):
        dproj, db = _attn_bwd_call(proj, bias, a, da, lse, g, dproj)
        dbs.append(db)
    G["rel_bias"] = _bias_grad_call(jnp.concatenate(dbs, axis=0), buckets)
    dproj = on_grad("w_in", functools.partial(_grad_half_matmul, h1, "proj_in_dw"), carry=dproj)
    dh1 = _matmul(dproj, P["w_in"], "nt", "proj_in_dx")
    dh1 = on_grad(None, None, carry=dh1)
    grad_x, G["norm_mix_pre"] = _in_bwd_call(xs, P["norm_mix_pre"], dh1, dx1)
    return loss_tile, grad_x, G


def kernel(x, w_in, b_gate, rel_bias, w_attn_out, conv_dw_w, conv_dw_b, conv_ln_g, conv_ln_b, conv_pw_w, w_out, norm_mix_pre, norm_mix_post, norm_ffn_pre, norm_ffn_post, w_up, ffn_conv_w, ffn_conv_b, w_down, loss_target, m_w_in, m_b_gate, m_rel_bias, m_w_attn_out, m_conv_dw_w, m_conv_dw_b, m_conv_ln_g, m_conv_ln_b, m_conv_pw_w, m_w_out, m_norm_mix_pre, m_norm_mix_post, m_norm_ffn_pre, m_norm_ffn_post, m_w_up, m_ffn_conv_w, m_ffn_conv_b, m_w_down, v_w_in, v_b_gate, v_rel_bias, v_w_attn_out, v_conv_dw_w, v_conv_dw_b, v_conv_ln_g, v_conv_ln_b, v_conv_pw_w, v_w_out, v_norm_mix_pre, v_norm_mix_post, v_norm_ffn_pre, v_norm_ffn_post, v_w_up, v_ffn_conv_w, v_ffn_conv_b, v_w_down):
    weights = dict(w_in=w_in, b_gate=b_gate, rel_bias=rel_bias, w_attn_out=w_attn_out, conv_dw_w=conv_dw_w, conv_dw_b=conv_dw_b,
                   conv_ln_g=conv_ln_g, conv_ln_b=conv_ln_b, conv_pw_w=conv_pw_w, w_out=w_out, norm_mix_pre=norm_mix_pre,
                   norm_mix_post=norm_mix_post, norm_ffn_pre=norm_ffn_pre, norm_ffn_post=norm_ffn_post, w_up=w_up,
                   ffn_conv_w=ffn_conv_w, ffn_conv_b=ffn_conv_b, w_down=w_down)
    m_in = dict(w_in=m_w_in, b_gate=m_b_gate, rel_bias=m_rel_bias, w_attn_out=m_w_attn_out, conv_dw_w=m_conv_dw_w,
                conv_dw_b=m_conv_dw_b, conv_ln_g=m_conv_ln_g, conv_ln_b=m_conv_ln_b, conv_pw_w=m_conv_pw_w, w_out=m_w_out,
                norm_mix_pre=m_norm_mix_pre, norm_mix_post=m_norm_mix_post, norm_ffn_pre=m_norm_ffn_pre,
                norm_ffn_post=m_norm_ffn_post, w_up=m_w_up, ffn_conv_w=m_ffn_conv_w, ffn_conv_b=m_ffn_conv_b, w_down=m_w_down)
    v_in = dict(w_in=v_w_in, b_gate=v_b_gate, rel_bias=v_rel_bias, w_attn_out=v_w_attn_out, conv_dw_w=v_conv_dw_w,
                conv_dw_b=v_conv_dw_b, conv_ln_g=v_conv_ln_g, conv_ln_b=v_conv_ln_b, conv_pw_w=v_conv_pw_w, w_out=v_w_out,
                norm_mix_pre=v_norm_mix_pre, norm_mix_post=v_norm_mix_post, norm_ffn_pre=v_norm_ffn_pre,
                norm_ffn_post=v_norm_ffn_post, w_up=v_w_up, ffn_conv_w=v_ffn_conv_w, ffn_conv_b=v_ffn_conv_b, w_down=v_w_down)
    names = list(weights)
    xi, yi, ci = _position()
    chip = 2 * xi + yi
    core_arr = jnp.reshape(ci, (1,)).astype(jnp.int32)

    xs = x[0]
    target = loss_target[0]
    S, D = xs.shape

    big = ["w_in", "w_attn_out", "conv_pw_w", "w_out", "w_up", "w_down"]
    row_sharded = ("conv_pw_w", "w_out", "w_down")
    natural = lambda k, g: g.reshape(-1, g.shape[2]) if k in row_sharded else g
    first_srcs = [w_in[0].astype(BF16), conv_dw_w[0], ffn_conv_w[0]]
    first_lands = [lax.empty((N_CHIPS,) + s.shape, s.dtype) for s in first_srcs]
    (first_hop,) = _split_start("gather_in_start", [(first_srcs, first_lands, 4 * len(first_srcs), _first_hop_copies)], core_arr)
    launched = first_hop["token"]
    late_sets = dict(mix=["w_attn_out", "conv_pw_w", "w_out"], up=["w_up"], down=["w_down"])
    late_groups = []
    for keys in late_sets.values():
        srcs = [(weights[k][0] + launched).astype(BF16) for k in keys]
        late_groups.append((srcs, [lax.empty((N_CHIPS,) + s.shape, BF16) for s in srcs], 4 * len(keys), _gather_copies))
    started = {}

    def late_weights(tag, after):
        if tag == "in":
            casts = [s for srcs, _, _, _ in late_groups for s in srcs]
            w_in_halves, dw4, fc4 = _split_wait("gather_in_wait", first_hop, _first_hop_copies, [after] + casts)[len(first_srcs):]
            second_hop, *late = _split_start("gather_in_pass_start", [([], [w_in_halves], 3, _second_hop_copies)] + late_groups, dw4)
            started.update(zip(late_sets, late))
            (w_in_full,) = _split_wait("gather_in_pass_wait", second_hop, _second_hop_copies, second_hop["tile"])
            return dict(w_in=w_in_full, conv_dw_w=jnp.concatenate(list(dw4), axis=1), ffn_conv_w=jnp.concatenate(list(fc4), axis=1))
        landed = _split_wait(f"gather_{tag}_wait", started[tag], _gather_copies, after)[len(late_sets[tag]):]
        return {k: natural(k, g) for k, g in zip(late_sets[tag], landed)}

    chip_core = jnp.stack([chip, ci]).astype(jnp.int32)
    exchanging, pending, second_half = {}, {}, {}

    held = []

    def launch(tag, after, carry=None):
        keys, groups, partial = [], [], {}
        for k in list(exchanging):
            st, copies = exchanging.pop(k)
            gk, r1 = _split_wait(f"sibling_exchange_wait_{k}", st, copies, after)
            if k in second_half:
                partial[k], s16 = second_half.pop(k)(init=r1)
            else:
                partial[k], s16 = _pair_sum_call(gk, r1, chip_core, f"pair_sum_{k}")
            keys.append(k)
            groups.append(([s16], [lax.empty((3,) + s16.shape[1:], BF16)], 3, _exchange_copies))
        fresh = [(k, copies) for k, _, copies in held]
        for _, g3, copies in held:
            rows = g3.shape[1] // 2 if copies is _sibling_copies else g3.shape[1]
            groups.append(([g3], [lax.empty((N_CHIPS, rows, g3.shape[2]), F32)], 1, copies))
        held.clear()
        begun = _split_start(f"grad_exchange_start_{tag}", groups, core_arr, carry, sibling_only=not keys)
        for k, st in zip(keys, begun):
            pending[k] = (partial[k], st)
        for (k, copies), st in zip(fresh, begun[len(keys):]):
            exchanging[k] = (st, copies)
        return begun[0]["carry"]

    others = [k for k in big if k != "w_in"]
    assembling = {}

    def on_grad(k, g, carry=None):
        if k is None:
            carried = launch("last", carry[:SUBLANES, :LANES], carry)
            assembling["others"] = assemble_start(others, carried, "others", carried)
            return assembling["others"]["carry"]
        if callable(g):
            theirs = g(jnp.stack([chip, 1 - ci]).astype(jnp.int32), carry)
            held.append((k, theirs, _sibling_whole_copies))
            carried = launch(k, theirs[0, :SUBLANES, :LANES], carry)
            second_half[k] = functools.partial(g, chip_core, carried)
            return carried
        held.append((k, g, _sibling_copies))
        if k in ("w_down", "w_out", "w_attn_out"):
            return carry
        return launch(k, g[0, :SUBLANES, :LANES], carry)

    def assemble_start(keys, after, tag, carry=None):
        halves = []
        for k in keys:
            s32, st = pending[k]
            recv2 = _split_wait(f"chip_exchange_wait_{k}", st, _exchange_copies, after)[1]
            halves.append(_chip_sum_call(s32, recv2, chip_core, f"chip_sum_{k}"))
        (st,) = _split_start(f"grad_assemble_start_{tag}", [([], halves, len(halves), _assemble_copies)], core_arr, carry,
                             sibling_only=True)
        return st

    def assemble_wait(keys, st, after, tag):
        return dict(zip(keys, _split_wait(f"grad_assemble_wait_{tag}", st, _assemble_copies, after)))

    P = dict(b_gate=b_gate, rel_bias=rel_bias, conv_dw_b=conv_dw_b, conv_ln_g=conv_ln_g, conv_ln_b=conv_ln_b,
             norm_mix_pre=norm_mix_pre + launched, norm_mix_post=norm_mix_post, norm_ffn_pre=norm_ffn_pre,
             norm_ffn_post=norm_ffn_post, ffn_conv_b=ffn_conv_b)
    loss_tile, grad_x, G = _local_step(xs, target, P, late_weights, on_grad)

    small = [k for k in names if k not in big]
    packed = _pack([loss_tile[:1]] + [G[k] for k in small])
    (allsum,) = _split_start("small_grad_allsum_start",
                             [([packed], [jnp.zeros((N_DEVICES,) + packed.shape, F32)], N_DEVICES - 1, _allsum_copies)], core_arr)

    reduced, grads, deltas, new_m, new_v = {}, {}, {}, {}, {}

    def update(keys):
        for k in keys:
            gk, d, mn, vn = _adamw_call(weights[k][0], reduced[k], m_in[k][0], v_in[k][0], f"adamw_{k}")
            grads[k], deltas[k], new_m[k], new_v[k] = gk[None], d[None], mn[None], vn[None]

    reduced.update(assemble_wait(others, assembling["others"], [allsum["tile"], grad_x], "others"))
    update(others)
    assembling["w_in"] = assemble_start(["w_in"], [deltas[k] for k in others], "w_in")

    me_chip = jnp.stack([4 * xi + 2 * yi + ci, chip]).astype(jnp.int32)
    mine, landed = _split_wait("small_grad_allsum_wait", allsum, _allsum_copies, assembling["w_in"]["tile"])
    piece_shapes = [(1, LANES)] + [(G[k].size // LANES, LANES) if k == "rel_bias" else G[k].shape for k in small]
    piece_cols = [0] + [weights[k].shape[2] if k in ("conv_dw_w", "ffn_conv_w") else 0 for k in small]
    loss_row, *summed = _ordered_sum_call(mine, landed, me_chip, piece_shapes, piece_cols)
    loss = loss_row[0, 0]
    for k, gsum in zip(small, summed):
        grads[k] = gsum.reshape(weights[k].shape)
    ds, mns, vns = _adamw_small_call([weights[k] for k in small], [grads[k] for k in small],
                                     [m_in[k] for k in small], [v_in[k] for k in small])
    deltas.update(zip(small, ds))
    new_m.update(zip(small, mns))
    new_v.update(zip(small, vns))
    reduced.update(assemble_wait(["w_in"], assembling["w_in"], list(ds), "w_in"))
    update(["w_in"])

    return (loss, grad_x[None], *[grads[k] for k in names], *[deltas[k] for k in names],
            *[new_m[k] for k in names], *[new_v[k] for k in names])
```

```python
import functools
import math

import jax
import jax.numpy as jnp
import numpy as np
from jax import lax
from jax.experimental import pallas as pl
from jax.experimental.pallas import tpu as pltpu

F32 = jnp.float32
BF16 = jnp.bfloat16
MESH = pl.DeviceIdType.MESH

HEAD_DIM = 128
HEADS_PER_GROUP = 4
DILATED_PATTERNS = ((128, 1), (512, 4), (2048, 16))
N_GROUPS = 3
N_HEADS = N_GROUPS * HEADS_PER_GROUP
SPAN = 128
GROUP_WIDTH = HEADS_PER_GROUP * HEAD_DIM
CONV_WIDTH = 31
FFN_CONV_WIDTH = 3
N_BUCKETS = 32
MAX_DISTANCE = 2048
RMS_EPS = 1e-6
LN_EPS = 1e-5
NEG_INF = -1e30
ADAM_LR = 0.001
ADAM_B1 = 0.9
ADAM_B2 = 0.999
ADAM_EPS = 1e-08
ADAM_WD = 0.01
ADAM_STEP = 10

LANES = 128
SUBLANES = 8
PACKED_ROWS = 16
ROW_TILE = 512
GATE_ROWS, GATE_COLS = 512, 512
TIME_BLOCK = 128
CONV_PAD = 32
FFN_PAD = 8
VMEM_LIMIT = 56 << 20


def _params(sem=None, vmem=None):
    kw = {}
    if sem is not None:
        kw["dimension_semantics"] = sem
    if vmem is not None:
        kw["vmem_limit_bytes"] = vmem
    return pltpu.CompilerParams(**kw)


def _pick(n, cands):
    for c in cands:
        if n % c == 0:
            return c
    return n


ELEMENTWISE_TILE_BYTES = 3 << 19


def _row_tile(rows, cols):
    for align in (16, SUBLANES):
        fits = [t for t in range(align, rows + 1, align) if rows % t == 0 and t * cols * 4 <= ELEMENTWISE_TILE_BYTES]
        if fits:
            return max(fits)
    return SUBLANES


N_CHIPS = 4
M_TILES = (1024, 1408, 512, 256, 128)
N_TILES = (1024, 512, 1408, 256, 128)
K_TILES = (2176, 2048, 1408, 1024, 512, 256, 128)


def _matmul(a, b, mode, name, out_shards=False, tm=None):
    assert a.dtype == BF16 and b.dtype == BF16, (name, a.dtype, b.dtype)
    b3 = b.ndim == 3
    tn = tk = None
    halves = None
    if mode == "nn":
        M, K = a.shape
        N = b.shape[-1] * (N_CHIPS if b3 else 1)
        tn = b.shape[-1] if b3 else None
    elif mode == "nt":
        if a.ndim == 3:
            halves = a.shape[2]
        M, K = a.shape[-2], a.shape[-1] * (a.shape[0] if a.ndim == 3 else 1)
        N = b.shape[-2]
        tk = b.shape[-1] if b3 else None
    else:
        if b3:
            halves = b.shape[2]
        K, M = a.shape
        N = b.shape[-1] * (b.shape[0] if b3 else 1)
        tn = N // N_CHIPS if out_shards else None
    tm = tm or _pick(M, M_TILES)
    tn = tn or _pick(N, N_TILES)
    tk = tk or _pick(K, K_TILES)
    nk = K // tk
    dn = {"nn": (((1,), (0,)), ((), ())), "nt": (((1,), (1,)), ((), ())), "tn": (((0,), (0,)), ((), ()))}[mode]

    def body(a_ref, b_ref, o_ref):
        if nk == 1:
            o_ref[...] = lax.dot_general(a_ref[...], b_ref[...], dn, preferred_element_type=F32)
        else:
            @pl.when(pl.program_id(2) == 0)
            def _():
                o_ref[...] = jnp.zeros_like(o_ref)

            o_ref[...] += lax.dot_general(a_ref[...], b_ref[...], dn, preferred_element_type=F32)

    if mode == "tn":
        a_spec = pl.BlockSpec((tk, tm), lambda i, j, k: (k, i))
    elif halves:
        per = halves // tk
        a_spec = pl.BlockSpec((None, tm, tk), lambda i, j, k: (k // per, i, k % per))
    else:
        a_spec = pl.BlockSpec((tm, tk), lambda i, j, k: (i, k))
    if mode == "nn":
        b_spec = pl.BlockSpec((None, tk, tn), lambda i, j, k: (j, k, 0)) if b3 else pl.BlockSpec((tk, tn), lambda i, j, k: (k, j))
    elif mode == "nt":
        b_spec = pl.BlockSpec((None, tn, tk), lambda i, j, k: (k, j, 0)) if b3 else pl.BlockSpec((tn, tk), lambda i, j, k: (j, k))
    elif halves:
        per = halves // tn
        b_spec = pl.BlockSpec((None, tk, tn), lambda i, j, k: (j // per, k, j % per))
    else:
        b_spec = pl.BlockSpec((tk, tn), lambda i, j, k: (k, j))
    if out_shards:
        out_spec = pl.BlockSpec((None, tm, tn), lambda i, j, k: (j, i, 0))
        out_shape = jax.ShapeDtypeStruct((N_CHIPS, M, tn), F32)
    else:
        out_spec = pl.BlockSpec((tm, tn), lambda i, j, k: (i, j))
        out_shape = jax.ShapeDtypeStruct((M, N), F32)
    return pl.pallas_call(
        body, name=name, grid=(M // tm, N // tn, nk),
        in_specs=[a_spec, b_spec], out_specs=out_spec, out_shape=out_shape,
        compiler_params=_params(("parallel", "parallel", "arbitrary"), VMEM_LIMIT),
    )(a, b)


def _grad_half_matmul(a, name, chip_half, b, init=None):
    K, M = a.shape
    parts = b.ndim == 3
    N = b.shape[-1] * (b.shape[0] if parts else 1)
    h, tn = M // 2, N // N_CHIPS
    summed = init is not None
    dn = (((0,), (0,)), ((), ()))

    def body(ch_ref, a_ref, b_ref, *rest):
        product = lax.dot_general(a_ref[...], b_ref[...], dn, preferred_element_type=F32)
        if not summed:
            rest[0][...] = product
            return
        init_ref, own_ref, sum16_ref = rest
        total = product + init_ref[...]
        sum16_ref[...] = total.astype(BF16)

        @pl.when(pl.program_id(0) == ch_ref[0])
        def _():
            own_ref[...] = total

    if parts:
        per = b.shape[2] // tn
        b_spec = pl.BlockSpec((None, K, tn), lambda j, ch_ref: (j // per, 0, j % per))
    else:
        b_spec = pl.BlockSpec((K, tn), lambda j, ch_ref: (0, j))
    shard_spec = pl.BlockSpec((None, h, tn), lambda j, ch_ref: (j, 0, 0))
    own_spec = pl.BlockSpec((h, tn), lambda j, ch_ref: (0, 0))
    shape = (N_CHIPS, h, tn)
    return pl.pallas_call(
        body, name=name + ("_mine" if summed else "_theirs"),
        grid_spec=pltpu.PrefetchScalarGridSpec(
            num_scalar_prefetch=1, grid=(N_CHIPS,),
            in_specs=[pl.BlockSpec((K, h), lambda j, ch_ref: (0, ch_ref[1])), b_spec] + [shard_spec] * summed,
            out_specs=[own_spec, shard_spec] if summed else shard_spec),
        out_shape=[jax.ShapeDtypeStruct((h, tn), F32), jax.ShapeDtypeStruct(shape, BF16)] if summed
        else jax.ShapeDtypeStruct(shape, F32),
        compiler_params=_params(("arbitrary",), VMEM_LIMIT),
    )(chip_half, a, b, *([init] if summed else []))


def _rms(x, g):
    r = lax.rsqrt(jnp.mean(x * x, axis=-1, keepdims=True) + RMS_EPS)
    return x * r * g


def _rms_bwd(x, g, dy):
    r = lax.rsqrt(jnp.mean(x * x, axis=-1, keepdims=True) + RMS_EPS)
    n = x * r
    dn = dy * g
    dx = r * (dn - n * jnp.mean(dn * n, axis=-1, keepdims=True))
    return dx, jnp.sum(dy * n, axis=0, keepdims=True)


def _sigmoid(x):
    return 1.0 / (1.0 + jnp.exp(-x))


_GELU_C = math.sqrt(2.0 / math.pi)


def _gelu(x):
    return 0.5 * x * (1.0 + jnp.tanh(_GELU_C * (x + 0.044715 * x * x * x)))


def _gelu_and_grad(x):
    x2 = x * x
    t = jnp.tanh(_GELU_C * x * (1.0 + 0.044715 * x2))
    half = 0.5 * (1.0 + t)
    return x * half, half + (0.5 * _GELU_C) * x * (1.0 - t * t) * (1.0 + (3.0 * 0.044715) * x2)


def _row_spec(width, col_block=0):
    return pl.BlockSpec((ROW_TILE, width), lambda i: (i, col_block))


def _vec_spec(width, col_block=0):
    return pl.BlockSpec((1, width), lambda i: (0, col_block))


def _accumulate(ref, part):
    @pl.when(pl.program_id(0) == 0)
    def _():
        ref[...] = part

    @pl.when(pl.program_id(0) > 0)
    def _():
        ref[...] += part


def _rms_fwd_call(x, g):
    S, D = x.shape

    def body(x_ref, g_ref, h_ref):
        h_ref[...] = _rms(x_ref[...], g_ref[...]).astype(BF16)

    return pl.pallas_call(
        body, name="rms_mix_pre", grid=(S // ROW_TILE,),
        in_specs=[_row_spec(D), _vec_spec(D)], out_specs=_row_spec(D),
        out_shape=jax.ShapeDtypeStruct((S, D), BF16),
        compiler_params=_params(("parallel",)),
    )(x, g)


def _ln_silu_call(c1, g, b):
    S, C = c1.shape

    def body(c_ref, g_ref, b_ref, o_ref):
        xv = c_ref[...]
        mu = jnp.mean(xv, axis=-1, keepdims=True)
        xc = xv - mu
        var = jnp.mean(xc * xc, axis=-1, keepdims=True)
        z = xc * lax.rsqrt(var + LN_EPS) * g_ref[...] + b_ref[...]
        o_ref[...] = (z * _sigmoid(z)).astype(BF16)

    return pl.pallas_call(
        body, name="conv_ln_silu", grid=(S // ROW_TILE,),
        in_specs=[_row_spec(C), _vec_spec(C), _vec_spec(C)], out_specs=_row_spec(C),
        out_shape=jax.ShapeDtypeStruct((S, C), BF16),
        compiler_params=_params(("parallel",)),
    )(c1, g, b)


def _ln_silu_bwd_call(c1, g, b, dy, w):
    S, C = c1.shape
    N = dy.shape[1]

    def body(c_ref, g_ref, b_ref, dy_ref, w_ref, dx_ref, dg_ref, db_ref):
        dc = lax.dot_general(dy_ref[...], w_ref[...], (((1,), (1,)), ((), ())), preferred_element_type=F32)
        xv = c_ref[...]
        mu = jnp.mean(xv, axis=-1, keepdims=True)
        xc = xv - mu
        rs = lax.rsqrt(jnp.mean(xc * xc, axis=-1, keepdims=True) + LN_EPS)
        xh = xc * rs
        z = xh * g_ref[...] + b_ref[...]
        sg = _sigmoid(z)
        dz = dc * (sg * (1.0 + z * (1.0 - sg)))
        dxh = dz * g_ref[...]
        dx_ref[...] = rs * (dxh - jnp.mean(dxh, axis=-1, keepdims=True) - xh * jnp.mean(dxh * xh, axis=-1, keepdims=True))
        _accumulate(dg_ref, jnp.sum(dz * xh, axis=0, keepdims=True))
        _accumulate(db_ref, jnp.sum(dz, axis=0, keepdims=True))

    return pl.pallas_call(
        body, name="conv_pw_dx_ln_silu_bwd", grid=(S // ROW_TILE,),
        in_specs=[_row_spec(C), _vec_spec(C), _vec_spec(C), _row_spec(N), pl.BlockSpec((C, N), lambda i: (0, 0))],
        out_specs=[_row_spec(C), _vec_spec(C), _vec_spec(C)],
        out_shape=[jax.ShapeDtypeStruct((S, C), F32), jax.ShapeDtypeStruct((1, C), F32), jax.ShapeDtypeStruct((1, C), F32)],
        compiler_params=_params(("arbitrary",), VMEM_LIMIT),
    )(c1, g, b, dy, w)


def _mix_call(proj, gate_col0, b_gate, y_a, y_c):
    S, D = y_a.shape
    w = GATE_COLS
    nc = D // w
    ga0, gc0 = gate_col0 // w, (gate_col0 + D) // w

    def body(ga_ref, gc_ref, ba_ref, bc_ref, ya_ref, yc_ref, o_ref):
        o_ref[...] = (_sigmoid(ga_ref[...] + ba_ref[...]) * ya_ref[...]
                      + _sigmoid(gc_ref[...] + bc_ref[...]) * yc_ref[...]).astype(BF16)

    tile = lambda off: pl.BlockSpec((GATE_ROWS, w), lambda i, j: (i, off + j))
    vec = lambda off: pl.BlockSpec((1, w), lambda i, j: (0, off + j))
    return pl.pallas_call(
        body, name="gate_mix", grid=(S // GATE_ROWS, nc),
        in_specs=[tile(ga0), tile(gc0), vec(0), vec(nc), tile(0), tile(0)],
        out_specs=tile(0), out_shape=jax.ShapeDtypeStruct((S, D), BF16),
        compiler_params=_params(("parallel", "parallel")),
    )(proj, proj, b_gate, b_gate, y_a, y_c)


def _window_stores(stage_ref, slot, dst_ref, rows, cols, sems):
    width = stage_ref.shape[-1]
    return [pltpu.make_async_copy(stage_ref.at[slot, p], dst_ref.at[rows, pl.ds(pl.multiple_of(c, LANES), width)], sems.at[slot, p])
            for p, c in enumerate(cols)]


def _staged_window_stores(stage_ref, dst_ref, sems, step, n_steps, rows, cols, fill):
    slot = step % 2
    copies = lambda s: _window_stores(stage_ref, s, dst_ref, rows, cols, sems)

    @pl.when(step >= 2)
    def _():
        for cp in copies(slot):
            cp.wait()

    fill(slot)
    for cp in copies(slot):
        cp.start()

    @pl.when(step == n_steps - 1)
    def _():
        for s in ([slot, 1 - slot] if n_steps > 1 else [slot]):
            for cp in copies(s):
                cp.wait()


def _mix_bwd_call(dmixed, proj, gate_col0, b_gate, y_a, y_c):
    S, D = y_a.shape
    w = GATE_COLS
    nc = D // w
    nr = S // GATE_ROWS
    ga0, gc0 = gate_col0 // w, (gate_col0 + D) // w

    def body(dm_ref, ga_ref, gc_ref, ba_ref, bc_ref, ya_ref, yc_ref, dya_ref, dyc_ref, dproj_ref, dba_ref, dbc_ref,
             stage_ref, sems):
        j, i = pl.program_id(0), pl.program_id(1)
        dm = dm_ref[...]
        sa = _sigmoid(ga_ref[...] + ba_ref[...])
        sc = _sigmoid(gc_ref[...] + bc_ref[...])
        dya_ref[...] = (dm * sa).astype(BF16)
        dyc_ref[...] = (dm * sc).astype(BF16)
        dga = dm * ya_ref[...] * sa * (1.0 - sa)
        dgc = dm * yc_ref[...] * sc * (1.0 - sc)

        def fill(slot):
            stage_ref[slot, 0] = dga.astype(BF16)
            stage_ref[slot, 1] = dgc.astype(BF16)

        rows = pl.ds(pl.multiple_of(i * GATE_ROWS, GATE_ROWS), GATE_ROWS)
        _staged_window_stores(stage_ref, dproj_ref, sems, j * nr + i, nc * nr, rows,
                              [gate_col0 + j * w, gate_col0 + D + j * w], fill)
        pa = jnp.sum(dga, axis=0, keepdims=True)
        pc = jnp.sum(dgc, axis=0, keepdims=True)

        @pl.when(i == 0)
        def _():
            dba_ref[...] = pa
            dbc_ref[...] = pc

        @pl.when(i > 0)
        def _():
            dba_ref[...] += pa
            dbc_ref[...] += pc

    tile = lambda off: pl.BlockSpec((GATE_ROWS, w), lambda j, i: (i, off + j))
    vec = lambda off: pl.BlockSpec((1, w), lambda j, i: (0, off + j))
    return pl.pallas_call(
        body, name="gate_mix_bwd", grid=(nc, nr),
        in_specs=[tile(0), tile(ga0), tile(gc0), vec(0), vec(nc), tile(0), tile(0)],
        out_specs=[tile(0), tile(0), ANY, vec(0), vec(0)],
        out_shape=[jax.ShapeDtypeStruct((S, D), BF16)] * 2 + [jax.ShapeDtypeStruct((S, proj.shape[1]), BF16)] + [
                   jax.ShapeDtypeStruct((1, D), F32), jax.ShapeDtypeStruct((1, D), F32)],
        scratch_shapes=[pltpu.VMEM((2, 2, GATE_ROWS, w), BF16), pltpu.SemaphoreType.DMA((2, 2))],
        compiler_params=_params(("arbitrary", "arbitrary")),
    )(dmixed, proj, proj, b_gate, b_gate, y_a, y_c)


def _res1_call(x, out, g_post, g_pre):
    S, D = x.shape

    def body(x_ref, o_ref, gp_ref, gq_ref, x1_ref, h2_ref):
        x1 = x_ref[...] + _rms(o_ref[...], gp_ref[...])
        x1_ref[...] = x1
        h2_ref[...] = _rms(x1, gq_ref[...]).astype(BF16)

    return pl.pallas_call(
        body, name="residual_mix", grid=(S // ROW_TILE,),
        in_specs=[_row_spec(D), _row_spec(D), _vec_spec(D), _vec_spec(D)],
        out_specs=[_row_spec(D), _row_spec(D)],
        out_shape=[jax.ShapeDtypeStruct((S, D), F32), jax.ShapeDtypeStruct((S, D), BF16)],
        compiler_params=_params(("parallel",)),
    )(x, out, g_post, g_pre)


def _ffn_down_loss_call(f, w_down, x1, g_post, target):
    S, K = f.shape
    D = w_down.shape[1]
    tk = _pick(K, K_TILES)
    nk = K // tk

    def body(f_ref, w_ref, x1_ref, g_ref, t_ref, loss_ref, dx_ref, dy_ref, dg_ref, y_ref):
        k = pl.program_id(1)
        product = lax.dot_general(f_ref[...], w_ref[...], (((1,), (0,)), ((), ())), preferred_element_type=F32)

        @pl.when(k == 0)
        def _():
            y_ref[...] = product

        @pl.when(k > 0)
        def _():
            y_ref[...] += product

        @pl.when(k == nk - 1)
        def _():
            yv, gv = y_ref[...], g_ref[...]
            err = x1_ref[...] + _rms(yv, gv) - t_ref[...]
            dx2 = err * (1.0 / D)
            dx_ref[...] = dx2
            dy, dg = _rms_bwd(yv, gv, dx2)
            dy_ref[...] = dy.astype(BF16)
            _accumulate(dg_ref, dg)
            part = 0.5 * jnp.sum(jnp.mean(err * err, axis=-1, keepdims=True), axis=0, keepdims=True)
            _accumulate(loss_ref, jnp.broadcast_to(part, (SUBLANES, LANES)))

    rows = pl.BlockSpec((ROW_TILE, D), lambda i, k: (i, 0))
    vec = pl.BlockSpec((1, D), lambda i, k: (0, 0))
    return pl.pallas_call(
        body, name="ffn_down_loss", grid=(S // ROW_TILE, nk),
        in_specs=[pl.BlockSpec((ROW_TILE, tk), lambda i, k: (i, k)), pl.BlockSpec((tk, D), lambda i, k: (k, 0)), rows, vec, rows],
        out_specs=[pl.BlockSpec((SUBLANES, LANES), lambda i, k: (0, 0)), rows, rows, vec],
        out_shape=[jax.ShapeDtypeStruct((SUBLANES, LANES), F32), jax.ShapeDtypeStruct((S, D), F32),
                   jax.ShapeDtypeStruct((S, D), BF16), jax.ShapeDtypeStruct((1, D), F32)],
        scratch_shapes=[pltpu.VMEM((ROW_TILE, D), F32)],
        compiler_params=_params(("arbitrary", "arbitrary"), VMEM_LIMIT),
    )(f, w_down, x1, g_post, target)


def _mid_bwd_call(x1, g_pre, dh2, dx2, out, g_post):
    S, D = x1.shape

    def body(x1_ref, gq_ref, dh_ref, dx2_ref, o_ref, gp_ref, dx1_ref, do_ref, dgq_ref, dgp_ref):
        d, dgq = _rms_bwd(x1_ref[...], gq_ref[...], dh_ref[...])
        dx1 = dx2_ref[...] + d
        dx1_ref[...] = dx1
        do, dgp = _rms_bwd(o_ref[...], gp_ref[...], dx1)
        do_ref[...] = do.astype(BF16)
        _accumulate(dgq_ref, dgq)
        _accumulate(dgp_ref, dgp)

    return pl.pallas_call(
        body, name="residual_mix_bwd", grid=(S // ROW_TILE,),
        in_specs=[_row_spec(D), _vec_spec(D), _row_spec(D), _row_spec(D), _row_spec(D), _vec_spec(D)],
        out_specs=[_row_spec(D), _row_spec(D), _vec_spec(D), _vec_spec(D)],
        out_shape=[jax.ShapeDtypeStruct((S, D), F32), jax.ShapeDtypeStruct((S, D), BF16)] + [jax.ShapeDtypeStruct((1, D), F32)] * 2,
        compiler_params=_params(("arbitrary",)),
    )(x1, g_pre, dh2, dx2, out, g_post)


def _in_bwd_call(x, g, dh1, dx1):
    S, D = x.shape

    def body(x_ref, g_ref, dh_ref, dx1_ref, gx_ref, dg_ref):
        d, dg = _rms_bwd(x_ref[...], g_ref[...], dh_ref[...])
        gx_ref[...] = dx1_ref[...] + d
        _accumulate(dg_ref, dg)

    return pl.pallas_call(
        body, name="rms_mix_pre_bwd", grid=(S // ROW_TILE,),
        in_specs=[_row_spec(D), _vec_spec(D), _row_spec(D), _row_spec(D)],
        out_specs=[_row_spec(D), _vec_spec(D)],
        out_shape=[jax.ShapeDtypeStruct((S, D), F32), jax.ShapeDtypeStruct((1, D), F32)],
        compiler_params=_params(("arbitrary",)),
    )(x, g, dh1, dx1)


def _bucket_table(dilation):
    qi = np.arange(SPAN)[:, None]
    ki = np.arange(2 * SPAN)[None, :]
    dist = np.maximum(qi + SPAN - ki, 0) * dilation
    max_exact = N_BUCKETS // 2
    d = np.maximum(dist, 1).astype(np.float64)
    large = max_exact + (np.log(d / max_exact) / math.log(MAX_DISTANCE / max_exact) * (N_BUCKETS - max_exact)).astype(np.int32)
    large = np.minimum(large, N_BUCKETS - 1)
    return np.where(dist < max_exact, dist, large).astype(np.int32)


def _bucket_tables():
    return jnp.asarray(np.stack([_bucket_table(r) for _, r in DILATED_PATTERNS]))


def _bias_table_call(rel_bias, buckets):
    def body(rb_ref, bk_ref, o_ref):
        for h in range(N_HEADS):
            bk = bk_ref[h // HEADS_PER_GROUP]

            def step(b, acc):
                return jnp.where(bk == b, rb_ref[b, h], acc)

            o_ref[h] = lax.fori_loop(0, N_BUCKETS, step, jnp.zeros((SPAN, 2 * SPAN), F32))

    return pl.pallas_call(
        body, name="rel_bias_table",
        in_specs=[pl.BlockSpec(memory_space=pltpu.SMEM), pl.BlockSpec(memory_space=pltpu.VMEM)],
        out_specs=pl.BlockSpec(memory_space=pltpu.VMEM),
        out_shape=jax.ShapeDtypeStruct((N_HEADS, SPAN, 2 * SPAN), F32),
    )(rel_bias, buckets)


def _bias_grad_call(dbias, buckets):
    def body(db_ref, bk_ref, o_ref, rows_ref):
        for h in range(N_HEADS):
            bk = bk_ref[h // HEADS_PER_GROUP]
            dv = db_ref[h]

            def step(b, carry):
                rows_ref[h, b] = jnp.sum(jnp.where(bk == b, dv, 0.0), axis=0, keepdims=True)
                return carry

            lax.fori_loop(0, N_BUCKETS, step, 0)
        o_ref[...] = jnp.sum(rows_ref[...], axis=-1, keepdims=True)

    out = pl.pallas_call(
        body, name="rel_bias_grad",
        in_specs=[pl.BlockSpec(memory_space=pltpu.VMEM), pl.BlockSpec(memory_space=pltpu.VMEM)],
        out_specs=pl.BlockSpec(memory_space=pltpu.VMEM),
        out_shape=jax.ShapeDtypeStruct((N_HEADS, N_BUCKETS, 1, 1), F32),
        scratch_shapes=[pltpu.VMEM((N_HEADS, N_BUCKETS, 1, 2 * SPAN), F32)],
    )(dbias, buckets)
    return out.reshape(N_HEADS, N_BUCKETS).T


def _dot_nt(a, b):
    return lax.dot_general(a, b, (((1,), (1,)), ((), ())), preferred_element_type=F32)


def _dot_nn(a, b):
    return lax.dot_general(a, b, (((1,), (0,)), ((), ())), preferred_element_type=F32)


def _dot_tn(a, b):
    return lax.dot_general(a, b, (((0,), (0,)), ((), ())), preferred_element_type=F32)


def _band_masks(n, nb):
    qi = lax.broadcasted_iota(jnp.int32, (SPAN, SPAN), 0)
    ki = lax.broadcasted_iota(jnp.int32, (SPAN, SPAN), 1)
    prev_ok = jnp.logical_and(ki >= qi, n > 0)
    cur_ok = ki <= qi
    next_ok = jnp.logical_and(ki >= qi, n < nb - 1)
    return prev_ok, cur_ok, next_ok


def _wide_band_mask(n):
    qi = lax.broadcasted_iota(jnp.int32, (SPAN, 2 * SPAN), 0)
    ki = lax.broadcasted_iota(jnp.int32, (SPAN, 2 * SPAN), 1)
    prev_ok = jnp.logical_and(jnp.logical_and(ki < SPAN, ki >= qi), n > 0)
    cur_ok = jnp.logical_and(ki >= SPAN, ki - SPAN <= qi)
    return jnp.logical_or(prev_ok, cur_ok)


def _attn_plan(S, group):
    r = DILATED_PATTERNS[group][1]
    hp, per = (HEADS_PER_GROUP, 1) if r == 1 else (2, 4)
    return r, S // (r * SPAN), hp, per


def _residue_rows(rho, r):
    return slice(None) if r == 1 else pl.ds(rho, SPAN, stride=r)


def _for_residues(r, per, fn):
    if r == per:
        for u in range(per):
            fn(u)
        return

    def step(i, carry):
        for u in range(per):
            fn(i * per + u)
        return carry

    lax.fori_loop(0, r // per, step, 0)


def _attn_fwd_call(proj, bias, group, merge_with=None):
    S = proj.shape[0]
    r, nb, hp, per = _attn_plan(S, group)
    scale = HEAD_DIM ** -0.5
    kinds = ("q", "kp", "kc", "vp", "vc") if nb > 1 else ("q", "kc", "vc")
    merge = merge_with is not None
    assert not merge or (r == 1 and hp == HEADS_PER_GROUP and len(merge_with) == 4)

    per_kind = _refs_per_kind(r, hp)

    def body(*refs):
        ins = {kind: refs[i * per_kind:(i + 1) * per_kind] for i, kind in enumerate(kinds)}
        b_ref, *rest = refs[len(kinds) * per_kind:]
        if merge:
            o2_ref, s2_ref, o3_ref, s3_ref, a_ref, ab_ref, lse_ref = rest
        else:
            o_ref, lse_ref = rest
        n = pl.program_id(1)
        prev_ok, cur_ok, _ = _band_masks(n, nb)

        band_ok = _wide_band_mask(n) if nb > 1 else cur_ok

        def residue(rho):
            rows = _residue_rows(rho, r)
            for j in range(hp):
                get = lambda kind: _head_rows(ins[kind], j, rows, r).astype(BF16)
                q = get("q")
                if nb > 1:
                    keys, vals, bias_j = jnp.concatenate([get("kp"), get("kc")], axis=0), jnp.concatenate([get("vp"), get("vc")], axis=0), b_ref[j]
                else:
                    keys, vals, bias_j = get("kc"), get("vc"), b_ref[j, :, SPAN:]
                s = jnp.where(band_ok, _dot_nt(q, keys) * scale + bias_j, NEG_INF)
                m = jnp.max(s, axis=-1, keepdims=True)
                p = jnp.exp(s - m)
                den = jnp.sum(p, axis=-1, keepdims=True)
                o1 = _dot_nn(p.astype(BF16), vals) / den
                s1 = jnp.broadcast_to(m + jnp.log(den), (SPAN, HEAD_DIM))
                if not merge:
                    o_ref[j, rows, :] = o1
                    lse_ref[j, rows, :] = s1
                    continue
                sl = slice(j * HEAD_DIM, (j + 1) * HEAD_DIM)
                s2, s3 = s2_ref[j], s3_ref[j]
                mx = jnp.maximum(jnp.maximum(s1, s2), s3)
                w1 = jnp.exp(s1 - mx)
                w2 = jnp.exp(s2 - mx)
                w3 = jnp.exp(s3 - mx)
                total = w1 + w2 + w3
                merged = (w1 * o1 + w2 * o2_ref[j] + w3 * o3_ref[j]) / total
                a_ref[:, sl] = merged
                ab_ref[:, sl] = merged.astype(BF16)
                lse_ref[:, sl] = mx + jnp.log(total)

        _for_residues(r, per, residue)

    in_specs = [_head_spec(r, nb, hp, kind, group, jj) for kind in kinds for jj in range(per_kind)]
    in_specs.append(pl.BlockSpec((hp, SPAN, 2 * SPAN), lambda j, n: (group * (HEADS_PER_GROUP // hp) + j, 0, 0)))
    out = pl.BlockSpec((hp, r * SPAN, HEAD_DIM), lambda j, n: (j, n, 0))
    operands = [proj] * (len(in_specs) - 1) + [bias]
    if merge:
        in_specs += [out] * 4
        operands += list(merge_with)
        rows = pl.BlockSpec((SPAN, GROUP_WIDTH), lambda j, n: (n, 0))
        out_specs = [rows] * 3
        out_shape = [jax.ShapeDtypeStruct((S, GROUP_WIDTH), dt) for dt in (F32, BF16, F32)]
    else:
        out_specs = [out] * 2
        out_shape = [jax.ShapeDtypeStruct((HEADS_PER_GROUP, S, HEAD_DIM), F32)] * 2
    return pl.pallas_call(
        body, name=f"attn_fwd_g{group}", grid=(HEADS_PER_GROUP // hp, nb),
        in_specs=in_specs, out_specs=out_specs, out_shape=out_shape,
        compiler_params=_params(("parallel", "parallel"), VMEM_LIMIT),
    )(*operands)


_PROJ_PART = dict(q=0, qn=0, kp=1, kc=1, vp=2, vc=2)


def _refs_per_kind(r, hp):
    return 1 if r == 1 else hp


def _head_rows(refs, j, rows, r):
    return refs[0][:, j * HEAD_DIM:(j + 1) * HEAD_DIM] if r == 1 else refs[j][rows, :]


def _head_spec(r, nb, hp, kind, group, jj):
    if kind in _PROJ_PART:
        base = (_PROJ_PART[kind] * N_GROUPS + group) * HEADS_PER_GROUP
    else:
        base = 0
    if kind.endswith("p"):
        row = lambda n: jnp.maximum(n - 1, 0)
    elif kind.endswith("n"):
        row = lambda n: jnp.minimum(n + 1, nb - 1)
    else:
        row = lambda n: n
    if r == 1:
        return pl.BlockSpec((SPAN, hp * HEAD_DIM), lambda j, n: (row(n), base // hp + j))
    return pl.BlockSpec((r * SPAN, HEAD_DIM), lambda j, n: (row(n), base + j * hp + jj))


def _attn_bwd_call(proj, bias, a, da, lse, group, dproj):
    S = proj.shape[0]
    r, nb, hp, per = _attn_plan(S, group)
    scale = HEAD_DIM ** -0.5
    kinds = ("q", "qn", "kp", "kc", "vp", "vc", "da", "dan", "lse", "lsen", "a", "an") if nb > 1 else ("q", "kc", "vc", "da", "lse", "a")
    source = dict(da=da, dan=da, lse=lse, lsen=lse, a=a, an=a)

    per_kind = _refs_per_kind(r, hp)
    per_group = HEADS_PER_GROUP // hp
    block_rows = r * SPAN

    def body(*refs):
        ins = {kind: refs[i * per_kind:(i + 1) * per_kind] for i, kind in enumerate(kinds)}
        b_ref, _, dproj_ref, db_ref, stage_ref, sems = refs[len(kinds) * per_kind:][:6]
        strided_ref = None if r == 1 else refs[-1]
        jg, n = pl.program_id(0), pl.program_id(1)
        prev_ok, cur_ok, next_ok = _band_masks(n, nb)

        @pl.when(n == 0)
        def _():
            db_ref[...] = jnp.zeros_like(db_ref)

        band_ok = _wide_band_mask(n) if nb > 1 else cur_ok

        def fill(slot):
            def put(part, j, rows, value):
                if r == 1:
                    stage_ref[slot, part, :, j * HEAD_DIM:(j + 1) * HEAD_DIM] = value.astype(BF16)
                else:
                    strided_ref[part, j, rows, :] = value

            _for_residues(r, per, functools.partial(residue, put))
            if r > 1:
                for part in range(3):
                    for j in range(hp):
                        for t0 in range(0, block_rows, ROW_TILE):
                            stage_ref[slot, part, t0:t0 + ROW_TILE, j * HEAD_DIM:(j + 1) * HEAD_DIM] = (
                                strided_ref[part, j, t0:t0 + ROW_TILE, :].astype(BF16))

        def residue(put, rho):
            rows = _residue_rows(rho, r)
            for j in range(hp):
                get = lambda kind: _head_rows(ins[kind], j, rows, r)
                q = get("q").astype(BF16)
                kc = get("kc").astype(BF16)
                vc = get("vc").astype(BF16)
                da_q = get("da")
                dav = da_q.astype(BF16)
                lse_q = get("lse")
                dl_q = jnp.sum(get("a") * da_q, axis=-1, keepdims=True)
                if nb == 1:
                    pc = jnp.exp(jnp.where(cur_ok, _dot_nt(q, kc) * scale + b_ref[j, :, SPAN:], NEG_INF) - lse_q)
                    dsc = pc * (_dot_nt(dav, vc) - dl_q)
                    dsc_b = dsc.astype(BF16)
                    dq = _dot_nn(dsc_b, kc)
                    dk = _dot_tn(dsc_b, q)
                    dv = _dot_tn(pc.astype(BF16), dav)
                    db_ref[j, :, SPAN:] += dsc
                else:
                    qn = get("qn").astype(BF16)
                    da_n = get("dan")
                    dan = da_n.astype(BF16)
                    keys = jnp.concatenate([get("kp").astype(BF16), kc], axis=0)
                    vals = jnp.concatenate([get("vp").astype(BF16), vc], axis=0)
                    wide = lambda t: jnp.concatenate([t, t], axis=1)
                    p = jnp.exp(jnp.where(band_ok, _dot_nt(q, keys) * scale + b_ref[j], NEG_INF) - wide(lse_q))
                    ds = p * (_dot_nt(dav, vals) - dl_q)
                    dq = _dot_nn(ds.astype(BF16), keys)
                    db_ref[j] += ds
                    pn = jnp.exp(jnp.where(next_ok, _dot_nt(qn, kc) * scale + b_ref[j, :, :SPAN], NEG_INF) - get("lsen"))
                    dsn = pn * (_dot_nt(dan, vc) - jnp.sum(get("an") * da_n, axis=-1, keepdims=True))
                    both = lambda cur_part, next_part: jnp.concatenate([cur_part.astype(BF16), next_part.astype(BF16)], axis=0)
                    dk = _dot_tn(both(ds[:, SPAN:], dsn), jnp.concatenate([q, qn], axis=0))
                    dv = _dot_tn(both(p[:, SPAN:], pn), jnp.concatenate([dav, dan], axis=0))
                put(0, j, rows, dq * scale)
                put(1, j, rows, dk * scale)
                put(2, j, rows, dv)

        cols = [(part * N_GROUPS + group) * GROUP_WIDTH + jg * (hp * HEAD_DIM) for part in range(3)]
        rows = pl.ds(pl.multiple_of(n * block_rows, SPAN), block_rows)
        _staged_window_stores(stage_ref, dproj_ref, sems, jg * nb + n, per_group * nb, rows, cols, fill)

    band = (hp, SPAN, 2 * SPAN)
    in_specs = [_head_spec(r, nb, hp, kind, group, jj) for kind in kinds for jj in range(per_kind)]
    in_specs += [pl.BlockSpec(band, lambda j, n: (group * per_group + j, 0, 0)), ANY]
    operands = [source.get(kind, proj) for kind in kinds for _ in range(per_kind)] + [bias, dproj]
    scratch = [pltpu.VMEM((2, 3, block_rows, hp * HEAD_DIM), BF16), pltpu.SemaphoreType.DMA((2, 3))]
    if r > 1:
        scratch.append(pltpu.VMEM((3, hp, block_rows, HEAD_DIM), F32))
    return pl.pallas_call(
        body, name=f"attn_bwd_g{group}", grid=(per_group, nb),
        in_specs=in_specs,
        out_specs=[ANY, pl.BlockSpec(band, lambda j, n: (j, 0, 0))],
        out_shape=[jax.ShapeDtypeStruct(dproj.shape, BF16), jax.ShapeDtypeStruct((HEADS_PER_GROUP, SPAN, 2 * SPAN), F32)],
        input_output_aliases={len(operands) - 1: 0},
        scratch_shapes=scratch,
        compiler_params=_params(("arbitrary", "arbitrary"), VMEM_LIMIT),
    )(*operands)


def _tap_rows(xpad_ref, t0, k, width, pad):
    return xpad_ref[pl.ds(t0 + (pad - (width - 1 - k)), TIME_BLOCK), :]


def _conv_block(xpad_ref, t0, w_ref, width, pad):
    acc = None
    for k in range(width):
        term = w_ref[k:k + 1, :] * _tap_rows(xpad_ref, t0, k, width, pad)
        acc = term if acc is None else acc + term
    return acc


def _conv_transpose_block(dpad_ref, t0, w_ref, width):
    acc = None
    for k in range(width):
        term = w_ref[k:k + 1, :] * dpad_ref[pl.ds(t0 + (width - 1 - k), TIME_BLOCK), :]
        acc = term if acc is None else acc + term
    return acc


def _conv_weight_grad(xpad_ref, t0, dy, dw_ref, width, pad):
    for k in range(width):
        dw_ref[k:k + 1, :] += jnp.sum(dy * _tap_rows(xpad_ref, t0, k, width, pad), axis=0, keepdims=True)


def _time_loop(S, step, skip_first=0, skip_last=0):
    def it(tb, carry):
        step(pl.multiple_of(tb * TIME_BLOCK, TIME_BLOCK))
        return carry

    lax.fori_loop(skip_first, S // TIME_BLOCK - skip_last, it, 0)


def _fill_head(head_ref, x_ref, pad):
    head_ref[0:pad, :] = jnp.zeros((pad, LANES), F32)
    head_ref[pad:, :] = x_ref[0:TIME_BLOCK, :]


def _fill_tail(tail_ref, x_ref, pad):
    S = x_ref.shape[0]
    tail_ref[0:TIME_BLOCK, :] = x_ref[S - TIME_BLOCK:S, :]
    tail_ref[TIME_BLOCK:, :] = jnp.zeros((pad, LANES), F32)


def _conv_fwd_call(proj, col0, w, b):
    S = proj.shape[0]
    C = w.shape[1]
    nt = C // LANES
    v0, g0 = col0 // LANES, (col0 + C) // LANES

    def body(val_ref, gate_ref, w_ref, b_ref, o_ref, pad_ref):
        pad_ref[0:CONV_PAD, :] = jnp.zeros((CONV_PAD, LANES), F32)
        pad_ref[CONV_PAD:, :] = val_ref[...] * _sigmoid(gate_ref[...])

        def step(t0):
            o_ref[pl.ds(t0, TIME_BLOCK), :] = _conv_block(pad_ref, t0, w_ref, CONV_WIDTH, CONV_PAD) + b_ref[...]

        _time_loop(S, step)

    seq = lambda off: pl.BlockSpec((S, LANES), lambda i: (0, off + i))
    return pl.pallas_call(
        body, name="conv_module", grid=(nt,),
        in_specs=[seq(v0), seq(g0), pl.BlockSpec((CONV_WIDTH, LANES), lambda i: (0, i)), pl.BlockSpec((1, LANES), lambda i: (0, i))],
        out_specs=seq(0), out_shape=jax.ShapeDtypeStruct((S, C), F32),
        scratch_shapes=[pltpu.VMEM((S + CONV_PAD, LANES), F32)],
        compiler_params=_params(("parallel",)),
    )(proj, proj, w, b)


def _conv_bwd_call(proj, col0, w, dc1, dproj):
    S = proj.shape[0]
    C = w.shape[1]
    nt = C // LANES
    v0, g0 = col0 // LANES, (col0 + C) // LANES

    def body(val_ref, gate_ref, w_ref, dy_ref, _, dproj_ref, dw_ref, db_ref, xpad_ref, tail_ref, dwacc_ref, stage_ref, sems):
        i = pl.program_id(0)
        xpad_ref[0:CONV_PAD, :] = jnp.zeros((CONV_PAD, LANES), F32)
        xpad_ref[CONV_PAD:, :] = val_ref[...] * _sigmoid(gate_ref[...])
        _fill_tail(tail_ref, dy_ref, CONV_PAD)
        dwacc_ref[...] = jnp.zeros_like(dwacc_ref)

        def fill(slot):
            def block(t0, dy_src, dy_t0):
                rows = pl.ds(t0, TIME_BLOCK)
                _conv_weight_grad(xpad_ref, t0, dy_ref[rows, :], dwacc_ref, CONV_WIDTH, CONV_PAD)
                dc0 = _conv_transpose_block(dy_src, dy_t0, w_ref, CONV_WIDTH)
                sg = _sigmoid(gate_ref[rows, :])
                stage_ref[slot, 0, rows, :] = (dc0 * sg).astype(BF16)
                stage_ref[slot, 1, rows, :] = (dc0 * val_ref[rows, :] * sg * (1.0 - sg)).astype(BF16)

            _time_loop(S, lambda t0: block(t0, dy_ref, t0), skip_last=1)
            block(S - TIME_BLOCK, tail_ref, 0)

        _staged_window_stores(stage_ref, dproj_ref, sems, i, nt, pl.ds(0, S),
                              [col0 + i * LANES, col0 + C + i * LANES], fill)
        dw_ref[...] = dwacc_ref[...]
        db_ref[...] = jnp.sum(dy_ref[...], axis=0, keepdims=True)

    seq = lambda off: pl.BlockSpec((S, LANES), lambda i: (0, off + i))
    return pl.pallas_call(
        body, name="conv_module_bwd", grid=(nt,),
        in_specs=[seq(v0), seq(g0), pl.BlockSpec((CONV_WIDTH, LANES), lambda i: (0, i)), seq(0), ANY],
        out_specs=[ANY, pl.BlockSpec((CONV_PAD, LANES), lambda i: (0, i)), pl.BlockSpec((1, LANES), lambda i: (0, i))],
        out_shape=[jax.ShapeDtypeStruct(dproj.shape, BF16),
                   jax.ShapeDtypeStruct((CONV_PAD, C), F32), jax.ShapeDtypeStruct((1, C), F32)],
        input_output_aliases={4: 0},
        scratch_shapes=[pltpu.VMEM((S + CONV_PAD, LANES), F32), pltpu.VMEM((TIME_BLOCK + CONV_PAD, LANES), F32),
                        pltpu.VMEM((CONV_PAD, LANES), F32),
                        pltpu.VMEM((2, 2, S, LANES), BF16), pltpu.SemaphoreType.DMA((2, 2))],
        compiler_params=_params(("arbitrary",)),
    )(proj, proj, w, dc1, dproj)


def _ffn_fwd_call(u, w, b):
    S, C2 = u.shape
    C = C2 // 2
    width = _pick(C, (2 * LANES, LANES))
    nt = C // width

    def body(ug_ref, uv_ref, wg_ref, wv_ref, bg_ref, bv_ref, f_ref, xg_ref, xv_ref):
        zeros = jnp.zeros((FFN_PAD, LANES), F32)
        for part in range(width // LANES):
            lanes = slice(part * LANES, (part + 1) * LANES)
            xg_ref[0:FFN_PAD, :] = zeros
            xv_ref[0:FFN_PAD, :] = zeros
            xg_ref[FFN_PAD:, :] = ug_ref[:, lanes]
            xv_ref[FFN_PAD:, :] = uv_ref[:, lanes]

            def step(t0, lanes=lanes):
                cg = _conv_block(xg_ref, t0, wg_ref.at[:, lanes], FFN_CONV_WIDTH, FFN_PAD) + bg_ref[:, lanes]
                cv = _conv_block(xv_ref, t0, wv_ref.at[:, lanes], FFN_CONV_WIDTH, FFN_PAD) + bv_ref[:, lanes]
                f_ref[pl.ds(t0, TIME_BLOCK), lanes] = (_gelu(cg) * cv).astype(BF16)

            _time_loop(S, step)

    seq = lambda off: pl.BlockSpec((S, width), lambda i: (0, off + i))
    wsp = lambda off: pl.BlockSpec((FFN_CONV_WIDTH, width), lambda i: (0, off + i))
    bsp = lambda off: pl.BlockSpec((1, width), lambda i: (0, off + i))
    return pl.pallas_call(
        body, name="ffn_conv_geglu", grid=(nt,),
        in_specs=[seq(0), seq(nt), wsp(0), wsp(nt), bsp(0), bsp(nt)],
        out_specs=seq(0), out_shape=jax.ShapeDtypeStruct((S, C), BF16),
        scratch_shapes=[pltpu.VMEM((FFN_PAD + S, LANES), F32)] * 2,
        compiler_params=_params(("parallel",)),
    )(u, u, w, w, b, b)


def _ffn_bwd_call(u, w, b, df):
    S, C2 = u.shape
    C = C2 // 2
    nt = C // LANES

    def body(ug_ref, uv_ref, wg_ref, wv_ref, bg_ref, bv_ref, df_ref,
             du_ref, dwg_ref, dwv_ref, dbg_ref, dbv_ref,
             hg_ref, hv_ref, dg_ref, dv_ref, dwg_acc, dwv_acc, dbg_acc, dbv_acc):
        zeros = jnp.zeros((FFN_PAD, LANES), F32)
        _fill_head(hg_ref, ug_ref, FFN_PAD)
        _fill_head(hv_ref, uv_ref, FFN_PAD)
        dg_ref[S:, :] = zeros
        dv_ref[S:, :] = zeros
        dwg_acc[...] = jnp.zeros_like(dwg_acc)
        dwv_acc[...] = jnp.zeros_like(dwv_acc)
        dbg_acc[...] = jnp.zeros_like(dbg_acc)
        dbv_acc[...] = jnp.zeros_like(dbv_acc)

        def first(t0, xg_ref, xv_ref, x_t0, pad):
            rows = pl.ds(t0, TIME_BLOCK)
            cg = _conv_block(xg_ref, x_t0, wg_ref, FFN_CONV_WIDTH, pad) + bg_ref[...]
            cv = _conv_block(xv_ref, x_t0, wv_ref, FFN_CONV_WIDTH, pad) + bv_ref[...]
            dfb = df_ref[rows, :]
            gelu, gelu_grad = _gelu_and_grad(cg)
            dcg = dfb * cv * gelu_grad
            dcv = dfb * gelu
            dg_ref[rows, :] = dcg
            dv_ref[rows, :] = dcv
            _conv_weight_grad(xg_ref, x_t0, dcg, dwg_acc, FFN_CONV_WIDTH, pad)
            _conv_weight_grad(xv_ref, x_t0, dcv, dwv_acc, FFN_CONV_WIDTH, pad)
            dbg_acc[...] += jnp.sum(dcg, axis=0, keepdims=True)
            dbv_acc[...] += jnp.sum(dcv, axis=0, keepdims=True)

        def second(t0):
            rows = pl.ds(t0, TIME_BLOCK)
            du_ref[0, rows, :] = _conv_transpose_block(dg_ref, t0, wg_ref, FFN_CONV_WIDTH).astype(BF16)
            du_ref[1, rows, :] = _conv_transpose_block(dv_ref, t0, wv_ref, FFN_CONV_WIDTH).astype(BF16)

        first(0, hg_ref, hv_ref, 0, FFN_PAD)
        _time_loop(S, lambda t0: first(t0, ug_ref, uv_ref, t0, 0), skip_first=1)
        _time_loop(S, second)
        dwg_ref[...] = dwg_acc[...]
        dwv_ref[...] = dwv_acc[...]
        dbg_ref[...] = dbg_acc[...]
        dbv_ref[...] = dbv_acc[...]

    seq = lambda off: pl.BlockSpec((S, LANES), lambda i: (0, off + i))
    wsp = lambda off: pl.BlockSpec((FFN_CONV_WIDTH, LANES), lambda i: (0, off + i))
    bsp = lambda off: pl.BlockSpec((1, LANES), lambda i: (0, off + i))
    return pl.pallas_call(
        body, name="ffn_conv_geglu_bwd", grid=(nt,),
        in_specs=[seq(0), seq(nt), wsp(0), wsp(nt), bsp(0), bsp(nt), seq(0)],
        out_specs=[pl.BlockSpec((2, S, LANES), lambda i: (0, 0, i)),
                   pl.BlockSpec((SUBLANES, LANES), lambda i: (0, i)), pl.BlockSpec((SUBLANES, LANES), lambda i: (0, i)),
                   bsp(0), bsp(0)],
        out_shape=[jax.ShapeDtypeStruct((2, S, C), BF16)] + [jax.ShapeDtypeStruct((SUBLANES, C), F32)] * 2
        + [jax.ShapeDtypeStruct((1, C), F32)] * 2,
        scratch_shapes=[pltpu.VMEM((FFN_PAD + TIME_BLOCK, LANES), F32)] * 2 + [pltpu.VMEM((S + FFN_PAD, LANES), F32)] * 2
        + [pltpu.VMEM((SUBLANES, LANES), F32)] * 2
        + [pltpu.VMEM((1, LANES), F32)] * 2,
        compiler_params=_params(("parallel",)),
    )(u, u, w, w, b, b, df)


def _adamw(w_ref, g_ref, m_ref, v_ref, d_ref, mo_ref, vo_ref):
    gv = g_ref[...]
    mn = ADAM_B1 * m_ref[...] + (1.0 - ADAM_B1) * gv
    vn = ADAM_B2 * v_ref[...] + (1.0 - ADAM_B2) * (gv * gv)
    mo_ref[...] = mn
    vo_ref[...] = vn
    m_hat = mn * (1.0 / (1.0 - ADAM_B1 ** ADAM_STEP))
    v_hat = vn * (1.0 / (1.0 - ADAM_B2 ** ADAM_STEP))
    d_ref[...] = -ADAM_LR * (m_hat / (jnp.sqrt(v_hat) + ADAM_EPS) + ADAM_WD * w_ref[...])


def _adamw_call(w, g, m, v, name):
    R, C = w.shape
    tr = _row_tile(R, C)

    def body(w_ref, g_ref, m_ref, v_ref, go_ref, d_ref, mo_ref, vo_ref):
        go_ref[...] = g_ref[...]
        _adamw(w_ref, g_ref, m_ref, v_ref, d_ref, mo_ref, vo_ref)

    spec = pl.BlockSpec((tr, C), lambda i: (i, 0))
    return pl.pallas_call(
        body, name=name, grid=(R // tr,),
        in_specs=[spec] * 4, out_specs=[spec] * 4,
        out_shape=[jax.ShapeDtypeStruct((R, C), F32)] * 4,
        compiler_params=_params(("parallel",)),
    )(w, g, m, v)


def _adamw_small_call(ws, gs, ms, vs):
    n = len(ws)

    def body(*refs):
        w_refs, g_refs, m_refs, v_refs, d_refs, mo_refs, vo_refs = (refs[i * n:(i + 1) * n] for i in range(7))
        for i in range(n):
            _adamw(w_refs[i], g_refs[i], m_refs[i], v_refs[i], d_refs[i], mo_refs[i], vo_refs[i])

    whole = pl.BlockSpec(memory_space=pltpu.VMEM)
    outs = pl.pallas_call(
        body, name="adamw_small",
        in_specs=[whole] * (4 * n), out_specs=[whole] * (3 * n),
        out_shape=[jax.ShapeDtypeStruct(w.shape, F32) for w in ws] * 3,
    )(*ws, *gs, *ms, *vs)
    return outs[:n], outs[n:2 * n], outs[2 * n:]


def _position():
    return lax.axis_index("x"), lax.axis_index("y"), lax.axis_index("c")


def _chip_peers(x, y):
    return [(x, 1 - y), (1 - x, y), (1 - x, 1 - y)]


def _half_rows(ref, core, rows):
    h = rows // 2
    start = pl.multiple_of(core * h, PACKED_ROWS)
    return ref.at[pl.ds(start, h), :] if len(ref.shape) == 2 else ref.at[:, pl.ds(start, h), :]


def _shard_half(ref, shard, core, rows):
    h = rows // 2
    return ref.at[shard, pl.ds(pl.multiple_of(core * h, PACKED_ROWS), h), :]


ANY = pl.BlockSpec(memory_space=pl.ANY)


def _first_hop_copies(srcs, lands):
    x, y, c = _position()
    chip = 2 * x + y
    targets = [(px, py, c) for px, py in _chip_peers(x, y)] + [(x, y, 1 - c)]
    rows = srcs[0].shape[0]
    out = []
    for i, (s, l) in enumerate(zip(srcs, lands)):
        for k, dev in enumerate(targets):
            if i == 0 and k < 3:
                out.append((_half_rows(s, c, rows), _shard_half(l, chip, c, rows), dev, k))
            else:
                out.append((s, l.at[chip], dev, len(targets) * i + k))
    return out


def _second_hop_copies(srcs, lands):
    x, y, c = _position()
    rows = lands[0].shape[1]
    out = []
    for k, (px, py) in enumerate(_chip_peers(x, y)):
        half = _shard_half(lands[0], 2 * px + py, c, rows)
        out.append((half, half, (x, y, 1 - c), k))
    return out


HBM_SPEC = pl.BlockSpec(memory_space=pltpu.HBM)
SEM_SPEC = pl.BlockSpec(memory_space=pltpu.SEMAPHORE)
DATAFLOW = pltpu.SideEffectType.DATAFLOW_SIDE_EFFECTING


def _in_hbm(a):
    return pltpu.with_memory_space_constraint(a, pltpu.HBM)


SIBLING_HANDSHAKE_IDS = dict(grad_exchange_start_w_up=1, grad_assemble_start_others=2, grad_assemble_start_w_in=3)


def _split_start(name, groups, after, carry=None, sibling_only=False):
    spans, arrays = [], []
    for srcs, lands, _, _ in groups:
        spans.append((len(arrays), len(srcs), len(lands)))
        arrays += list(srcs) + list(lands)
    if carry is not None:
        arrays.append(carry)
    na, ng = len(arrays), len(groups)

    def body(*refs):
        sems, token = refs[na + 1:na + 1 + 2 * ng], refs[-1]
        if sibling_only:
            x, y, c = _position()
            barrier = pltpu.get_barrier_semaphore()
            pl.semaphore_signal(barrier, inc=1, device_id=(x, y, 1 - c), device_id_type=MESH)
            pl.semaphore_wait(barrier, 1)
        for g, (_, _, _, copies) in enumerate(groups):
            off, ns, nl = spans[g]
            for src, dst, dev, idx in copies(refs[off:off + ns], refs[off + ns:off + ns + nl]):
                pltpu.make_async_remote_copy(src_ref=src, dst_ref=dst, send_sem=sems[2 * g].at[idx], recv_sem=sems[2 * g + 1].at[idx],
                                             device_id=dev, device_id_type=MESH).start()
        token[...] = jnp.zeros_like(token)

    outs = pl.pallas_call(
        body, name=name,
        in_specs=[HBM_SPEC] * na + [ANY],
        out_specs=[SEM_SPEC] * (2 * ng) + [HBM_SPEC] * na + [pl.BlockSpec(memory_space=pltpu.VMEM)],
        out_shape=[pltpu.SemaphoreType.DMA((n_sems,)) for _, _, n_sems, _ in groups for _ in range(2)]
        + [pltpu.HBM(a.shape, a.dtype) for a in arrays] + [jax.ShapeDtypeStruct((SUBLANES, LANES), F32)],
        input_output_aliases={i: 2 * ng + i for i in range(na)},
        compiler_params=pltpu.CompilerParams(has_side_effects=DATAFLOW,
                                             collective_id=SIBLING_HANDSHAKE_IDS[name] if sibling_only else None),
    )(*[_in_hbm(a) for a in arrays], after)
    started = []
    for g, (off, ns, nl) in enumerate(spans):
        thru = outs[2 * ng + off:2 * ng + off + ns + nl]
        started.append(dict(send=outs[2 * g], recv=outs[2 * g + 1], srcs=list(thru[:ns]), lands=list(thru[ns:]),
                            tile=outs[-1], token=outs[-1][0, 0], carry=None if carry is None else outs[2 * ng + na - 1]))
    return started


def _split_wait(name, started, copies, after):
    n, m = len(started["srcs"]), len(started["lands"])
    after = list(after) if isinstance(after, (list, tuple)) else [after]

    def body(*refs):
        src_refs, land_refs = refs[:n], refs[n:n + m]
        send_sem, recv_sem = refs[n + m], refs[n + m + 1]
        for src, dst, dev, idx in copies(src_refs, land_refs):
            cp = pltpu.make_async_remote_copy(src_ref=src, dst_ref=dst, send_sem=send_sem.at[idx], recv_sem=recv_sem.at[idx],
                                              device_id=dev, device_id_type=MESH)
            cp.wait_send()
            cp.wait_recv()

    arrays = started["srcs"] + started["lands"]
    outs = pl.pallas_call(
        body, name=name,
        in_specs=[HBM_SPEC] * (n + m) + [SEM_SPEC, SEM_SPEC] + [ANY] * len(after),
        out_specs=[HBM_SPEC] * (n + m),
        out_shape=[pltpu.HBM(a.shape, a.dtype) for a in arrays],
        input_output_aliases={i: i for i in range(n + m)},
        compiler_params=pltpu.CompilerParams(has_side_effects=DATAFLOW),
    )(*arrays, started["send"], started["recv"], *after)
    return list(outs)


def _gather_copies(srcs, lands):
    x, y, c = _position()
    chip = 2 * x + y
    targets = [(px, py, c) for px, py in _chip_peers(x, y)] + [(x, y, 1 - c)]
    return [(s, l.at[chip], dev, len(targets) * i + k) for i, (s, l) in enumerate(zip(srcs, lands)) for k, dev in enumerate(targets)]


def _sibling_copies(srcs, lands):
    x, y, c = _position()
    return [(_half_rows(srcs[0], 1 - c, srcs[0].shape[1]), lands[0], (x, y, 1 - c), 0)]


def _sibling_whole_copies(srcs, lands):
    x, y, c = _position()
    return [(srcs[0], lands[0], (x, y, 1 - c), 0)]


def _exchange_copies(srcs, lands):
    x, y, c = _position()
    return [(srcs[0].at[2 * px + py], lands[0].at[k], (px, py, c), k) for k, (px, py) in enumerate(_chip_peers(x, y))]


def _pair_sum_call(grad, recv, chip_core, name):
    _, h, B = recv.shape
    tr = _row_tile(h, B)

    def body(cc_ref, g_ref, r_ref, o_ref, ob_ref):
        s = g_ref[...] + r_ref[...]
        ob_ref[...] = s.astype(BF16)

        @pl.when(pl.program_id(1) == cc_ref[0])
        def _():
            o_ref[...] = s

    g_spec = pl.BlockSpec((None, tr, B), lambda i, q, cc_ref: (q, cc_ref[1] * (h // tr) + i, 0))
    spec = pl.BlockSpec((None, tr, B), lambda i, q, cc_ref: (q, i, 0))
    own_spec = pl.BlockSpec((tr, B), lambda i, q, cc_ref: (i, 0))
    return pl.pallas_call(
        body, name=name,
        grid_spec=pltpu.PrefetchScalarGridSpec(num_scalar_prefetch=1, grid=(h // tr, N_CHIPS), in_specs=[g_spec, spec],
                                               out_specs=[own_spec, spec]),
        out_shape=[jax.ShapeDtypeStruct((h, B), F32), jax.ShapeDtypeStruct(recv.shape, BF16)],
        compiler_params=_params(("parallel", "arbitrary")),
    )(chip_core, grad, recv)


def _chip_sum_call(partial, recv, chip_core, name):
    _, h, B = recv.shape
    tr = _row_tile(h, B)

    def body(cc_ref, p_ref, r_ref, o_ref):
        o_ref[...] = ((p_ref[...] + r_ref[0].astype(F32)) + r_ref[1].astype(F32)) + r_ref[2].astype(F32)

    return pl.pallas_call(
        body, name=name,
        grid_spec=pltpu.PrefetchScalarGridSpec(
            num_scalar_prefetch=1, grid=(h // tr,),
            in_specs=[pl.BlockSpec((tr, B), lambda i, cc_ref: (i, 0)),
                      pl.BlockSpec((3, tr, B), lambda i, cc_ref: (0, i, 0))],
            out_specs=pl.BlockSpec((tr, B), lambda i, cc_ref: (cc_ref[1] * (h // tr) + i, 0))),
        out_shape=jax.ShapeDtypeStruct((2 * h, B), F32),
        compiler_params=_params(("parallel",)),
    )(chip_core, partial, recv)


def _assemble_copies(srcs, lands):
    x, y, c = _position()
    out = []
    for i, land in enumerate(lands):
        half = _half_rows(land, c, land.shape[0])
        out.append((half, half, (x, y, 1 - c), i))
    return out


N_DEVICES = 8


def _allsum_copies(srcs, lands):
    x, y, c = _position()
    me = 4 * x + 2 * y + c
    out = []
    for k in range(1, N_DEVICES):
        peer = (1 - x if k & 4 else x, 1 - y if k & 2 else y, 1 - c if k & 1 else c)
        out.append((srcs[0], lands[0].at[me], peer, k - 1))
    return out


def _ordered_sum_call(mine, landed, me_chip, shapes, sharded_cols):
    rows = mine.shape[0]
    outs = [(s[0], n) if n else s for s, n in zip(shapes, sharded_cols)]

    def body(mc_ref, x_ref, l_ref, *refs):
        acc_ref = refs[-1]
        acc = jnp.where(mc_ref[0] == 0, x_ref[...], l_ref[0])
        for d in range(1, N_DEVICES):
            acc = acc + jnp.where(mc_ref[0] == d, x_ref[...], l_ref[d])
        acc_ref[...] = acc
        first = 0
        for o_ref, (r, c), n in zip(refs[:-1], shapes, sharded_cols):
            per_row = c // LANES

            def unpack(chip, o_ref=o_ref, r=r, n=n, per_row=per_row, first=first):
                for i in range(r):
                    for j in range((n or per_row * LANES) // LANES):
                        src = first + i * per_row + chip * ((n or 0) // LANES) + j
                        o_ref[i:i + 1, j * LANES:(j + 1) * LANES] = acc_ref[src:src + 1, :]

            if n:
                for q in range(N_CHIPS):
                    pl.when(mc_ref[1] == q)(functools.partial(unpack, q))
            else:
                unpack(0)
            first += r * per_row

    results = pl.pallas_call(
        body, name="small_grad_sum",
        in_specs=[pl.BlockSpec(memory_space=pltpu.SMEM), pl.BlockSpec(memory_space=pltpu.VMEM), pl.BlockSpec(memory_space=pltpu.VMEM)],
        out_specs=[pl.BlockSpec(memory_space=pltpu.VMEM)] * len(outs),
        out_shape=[jax.ShapeDtypeStruct(s, F32) for s in outs],
        scratch_shapes=[pltpu.VMEM((rows, LANES), F32)],
    )(me_chip, mine, landed)
    return results


def _pack(arrays):
    flat = jnp.concatenate([a.reshape(-1).astype(F32) for a in arrays])
    rows = -(-flat.shape[0] // LANES)
    rows = -(-rows // SUBLANES) * SUBLANES
    flat = jnp.pad(flat, (0, rows * LANES - flat.shape[0]))
    return flat.reshape(rows, LANES)


def _local_step(xs, target, P, late_weights, on_grad):
    S, D = xs.shape
    qkv_width = 3 * N_HEADS * HEAD_DIM
    glu_col0, gate_col0 = qkv_width, qkv_width + 2 * D
    shard_major = lambda g: g.reshape(N_CHIPS, g.shape[0] // N_CHIPS, g.shape[1])

    h1 = _rms_fwd_call(xs, P["norm_mix_pre"])
    buckets = _bucket_tables()
    bias = _bias_table_call(P["rel_bias"] + 0.0 * h1[0, 0].astype(F32), buckets)
    P = dict(P, **late_weights("in", bias))
    proj = _matmul(h1, P["w_in"], "nn", "proj_in")
    dilated = []
    for g in range(1, N_GROUPS):
        dilated += _attn_fwd_call(proj, bias, g)
    a, a_bf, lse = _attn_fwd_call(proj, bias, 0, merge_with=dilated)
    P = dict(P, **late_weights("mix", a_bf))
    y_a = _matmul(a_bf, P["w_attn_out"], "nn", "attn_out")
    c1 = _conv_fwd_call(proj, glu_col0, P["conv_dw_w"], P["conv_dw_b"])
    cact = _ln_silu_call(c1, P["conv_ln_g"], P["conv_ln_b"])
    y_c = _matmul(cact, P["conv_pw_w"], "nn", "conv_pw")
    mixed = _mix_call(proj, gate_col0, P["b_gate"], y_a, y_c)
    out = _matmul(mixed, P["w_out"], "nn", "mix_out")
    x1, h2 = _res1_call(xs, out, P["norm_mix_post"], P["norm_ffn_pre"])
    P = dict(P, **late_weights("up", h2))
    u = _matmul(h2, P["w_up"], "nn", "ffn_up")
    f = _ffn_fwd_call(u, P["ffn_conv_w"], P["ffn_conv_b"])
    P = dict(P, **late_weights("down", f))
    loss_tile, dx2, dyff, dg_ffn_post = _ffn_down_loss_call(f, P["w_down"], x1, P["norm_ffn_post"], target)

    G = {}
    G["norm_ffn_post"] = dg_ffn_post
    on_grad("w_down", shard_major(_matmul(f, dyff, "tn", "ffn_down_dw")))
    df = _matmul(dyff, P["w_down"], "nt", "ffn_down_dx")
    du, dwg, dwv, dbg, dbv = _ffn_bwd_call(u, P["ffn_conv_w"], P["ffn_conv_b"], df)
    G["ffn_conv_w"] = jnp.concatenate([dwg[:FFN_CONV_WIDTH], dwv[:FFN_CONV_WIDTH]], axis=1)
    G["ffn_conv_b"] = jnp.concatenate([dbg, dbv], axis=1)
    du = on_grad("w_up", functools.partial(_grad_half_matmul, h2, "ffn_up_dw"), carry=du)
    dh2 = _matmul(du, P["w_up"], "nt", "ffn_up_dx")
    dx1, dout, G["norm_ffn_pre"], G["norm_mix_post"] = _mid_bwd_call(x1, P["norm_ffn_pre"], dh2, dx2, out, P["norm_mix_post"])
    on_grad("w_out", shard_major(_matmul(mixed, dout, "tn", "mix_out_dw")))
    dmixed = _matmul(dout, P["w_out"], "nt", "mix_out_dx")
    dya, dyc, dproj, dba, dbc = _mix_bwd_call(dmixed, proj, gate_col0, P["b_gate"], y_a, y_c)
    G["b_gate"] = jnp.concatenate([dba, dbc], axis=1)
    on_grad("w_attn_out", _matmul(a_bf, dya, "tn", "attn_out_dw", out_shards=True))
    dyc = on_grad("conv_pw_w", shard_major(_matmul(cact, dyc, "tn", "conv_pw_dw")), carry=dyc)
    da = _matmul(dya, P["w_attn_out"], "nt", "attn_out_dx")
    dc1, G["conv_ln_g"], G["conv_ln_b"] = _ln_silu_bwd_call(c1, P["conv_ln_g"], P["conv_ln_b"], dyc, P["conv_pw_w"])
    dproj, dw_dw, G["conv_dw_b"] = _conv_bwd_call(proj, glu_col0, P["conv_dw_w"], dc1, dproj)
    G["conv_dw_w"] = dw_dw[:CONV_WIDTH]
    dbs = []
    for g in range(N_GROUPS):
        dproj, db = _attn_bwd_call(proj, bias, a, da, lse, g, dproj)
        dbs.append(db)
    G["rel_bias"] = _bias_grad_call(jnp.concatenate(dbs, axis=0), buckets)
    dproj = on_grad("w_in", functools.partial(_grad_half_matmul, h1, "proj_in_dw"), carry=dproj)
    dh1 = _matmul(dproj, P["w_in"], "nt", "proj_in_dx")
    dh1 = on_grad(None, None, carry=dh1)
    grad_x, G["norm_mix_pre"] = _in_bwd_call(xs, P["norm_mix_pre"], dh1, dx1)
    return loss_tile, grad_x, G


def kernel(x, w_in, b_gate, rel_bias, w_attn_out, conv_dw_w, conv_dw_b, conv_ln_g, conv_ln_b, conv_pw_w, w_out, norm_mix_pre, norm_mix_post, norm_ffn_pre, norm_ffn_post, w_up, ffn_conv_w, ffn_conv_b, w_down, loss_target, m_w_in, m_b_gate, m_rel_bias, m_w_attn_out, m_conv_dw_w, m_conv_dw_b, m_conv_ln_g, m_conv_ln_b, m_conv_pw_w, m_w_out, m_norm_mix_pre, m_norm_mix_post, m_norm_ffn_pre, m_norm_ffn_post, m_w_up, m_ffn_conv_w, m_ffn_conv_b, m_w_down, v_w_in, v_b_gate, v_rel_bias, v_w_attn_out, v_conv_dw_w, v_conv_dw_b, v_conv_ln_g, v_conv_ln_b, v_conv_pw_w, v_w_out, v_norm_mix_pre, v_norm_mix_post, v_norm_ffn_pre, v_norm_ffn_post, v_w_up, v_ffn_conv_w, v_ffn_conv_b, v_w_down):
    weights = dict(w_in=w_in, b_gate=b_gate, rel_bias=rel_bias, w_attn_out=w_attn_out, conv_dw_w=conv_dw_w, conv_dw_b=conv_dw_b,
                   conv_ln_g=conv_ln_g, conv_ln_b=conv_ln_b, conv_pw_w=conv_pw_w, w_out=w_out, norm_mix_pre=norm_mix_pre,
                   norm_mix_post=norm_mix_post, norm_ffn_pre=norm_ffn_pre, norm_ffn_post=norm_ffn_post, w_up=w_up,
                   ffn_conv_w=ffn_conv_w, ffn_conv_b=ffn_conv_b, w_down=w_down)
    m_in = dict(w_in=m_w_in, b_gate=m_b_gate, rel_bias=m_rel_bias, w_attn_out=m_w_attn_out, conv_dw_w=m_conv_dw_w,
                conv_dw_b=m_conv_dw_b, conv_ln_g=m_conv_ln_g, conv_ln_b=m_conv_ln_b, conv_pw_w=m_conv_pw_w, w_out=m_w_out,
                norm_mix_pre=m_norm_mix_pre, norm_mix_post=m_norm_mix_post, norm_ffn_pre=m_norm_ffn_pre,
                norm_ffn_post=m_norm_ffn_post, w_up=m_w_up, ffn_conv_w=m_ffn_conv_w, ffn_conv_b=m_ffn_conv_b, w_down=m_w_down)
    v_in = dict(w_in=v_w_in, b_gate=v_b_gate, rel_bias=v_rel_bias, w_attn_out=v_w_attn_out, conv_dw_w=v_conv_dw_w,
                conv_dw_b=v_conv_dw_b, conv_ln_g=v_conv_ln_g, conv_ln_b=v_conv_ln_b, conv_pw_w=v_conv_pw_w, w_out=v_w_out,
                norm_mix_pre=v_norm_mix_pre, norm_mix_post=v_norm_mix_post, norm_ffn_pre=v_norm_ffn_pre,
                norm_ffn_post=v_norm_ffn_post, w_up=v_w_up, ffn_conv_w=v_ffn_conv_w, ffn_conv_b=v_ffn_conv_b, w_down=v_w_down)
    names = list(weights)
    xi, yi, ci = _position()
    chip = 2 * xi + yi
    core_arr = jnp.reshape(ci, (1,)).astype(jnp.int32)

    xs = x[0]
    target = loss_target[0]
    S, D = xs.shape

    big = ["w_in", "w_attn_out", "conv_pw_w", "w_out", "w_up", "w_down"]
    row_sharded = ("conv_pw_w", "w_out", "w_down")
    natural = lambda k, g: g.reshape(-1, g.shape[2]) if k in row_sharded else g
    first_srcs = [w_in[0].astype(BF16), conv_dw_w[0], ffn_conv_w[0]]
    first_lands = [lax.empty((N_CHIPS,) + s.shape, s.dtype) for s in first_srcs]
    (first_hop,) = _split_start("gather_in_start", [(first_srcs, first_lands, 4 * len(first_srcs), _first_hop_copies)], core_arr)
    launched = first_hop["token"]
    late_sets = dict(mix=["w_attn_out", "conv_pw_w", "w_out"], up=["w_up"], down=["w_down"])
    late_groups = []
    for keys in late_sets.values():
        srcs = [(weights[k][0] + launched).astype(BF16) for k in keys]
        late_groups.append((srcs, [lax.empty((N_CHIPS,) + s.shape, BF16) for s in srcs], 4 * len(keys), _gather_copies))
    started = {}

    def late_weights(tag, after):
        if tag == "in":
            casts = [s for srcs, _, _, _ in late_groups for s in srcs]
            w_in_halves, dw4, fc4 = _split_wait("gather_in_wait", first_hop, _first_hop_copies, [after] + casts)[len(first_srcs):]
            second_hop, *late = _split_start("gather_in_pass_start", [([], [w_in_halves], 3, _second_hop_copies)] + late_groups, dw4)
            started.update(zip(late_sets, late))
            (w_in_full,) = _split_wait("gather_in_pass_wait", second_hop, _second_hop_copies, second_hop["tile"])
            return dict(w_in=w_in_full, conv_dw_w=jnp.concatenate(list(dw4), axis=1), ffn_conv_w=jnp.concatenate(list(fc4), axis=1))
        landed = _split_wait(f"gather_{tag}_wait", started[tag], _gather_copies, after)[len(late_sets[tag]):]
        return {k: natural(k, g) for k, g in zip(late_sets[tag], landed)}

    chip_core = jnp.stack([chip, ci]).astype(jnp.int32)
    exchanging, pending, second_half = {}, {}, {}

    held = []

    def launch(tag, after, carry=None):
        keys, groups, partial = [], [], {}
        for k in list(exchanging):
            st, copies = exchanging.pop(k)
            gk, r1 = _split_wait(f"sibling_exchange_wait_{k}", st, copies, after)
            if k in second_half:
                partial[k], s16 = second_half.pop(k)(init=r1)
            else:
                partial[k], s16 = _pair_sum_call(gk, r1, chip_core, f"pair_sum_{k}")
            keys.append(k)
            groups.append(([s16], [lax.empty((3,) + s16.shape[1:], BF16)], 3, _exchange_copies))
        fresh = [(k, copies) for k, _, copies in held]
        for _, g3, copies in held:
            rows = g3.shape[1] // 2 if copies is _sibling_copies else g3.shape[1]
            groups.append(([g3], [lax.empty((N_CHIPS, rows, g3.shape[2]), F32)], 1, copies))
        held.clear()
        begun = _split_start(f"grad_exchange_start_{tag}", groups, core_arr, carry, sibling_only=not keys)
        for k, st in zip(keys, begun):
            pending[k] = (partial[k], st)
        for (k, copies), st in zip(fresh, begun[len(keys):]):
            exchanging[k] = (st, copies)
        return begun[0]["carry"]

    others = [k for k in big if k != "w_in"]
    assembling = {}

    def on_grad(k, g, carry=None):
        if k is None:
            carried = launch("last", carry[:SUBLANES, :LANES], carry)
            assembling["others"] = assemble_start(others, carried, "others", carried)
            return assembling["others"]["carry"]
        if callable(g):
            theirs = g(jnp.stack([chip, 1 - ci]).astype(jnp.int32), carry)
            held.append((k, theirs, _sibling_whole_copies))
            carried = launch(k, theirs[0, :SUBLANES, :LANES], carry)
            second_half[k] = functools.partial(g, chip_core, carried)
            return carried
        held.append((k, g, _sibling_copies))
        if k in ("w_down", "w_out", "w_attn_out"):
            return carry
        return launch(k, g[0, :SUBLANES, :LANES], carry)

    def assemble_start(keys, after, tag, carry=None):
        halves = []
        for k in keys:
            s32, st = pending[k]
            recv2 = _split_wait(f"chip_exchange_wait_{k}", st, _exchange_copies, after)[1]
            halves.append(_chip_sum_call(s32, recv2, chip_core, f"chip_sum_{k}"))
        (st,) = _split_start(f"grad_assemble_start_{tag}", [([], halves, len(halves), _assemble_copies)], core_arr, carry,
                             sibling_only=True)
        return st

    def assemble_wait(keys, st, after, tag):
        return dict(zip(keys, _split_wait(f"grad_assemble_wait_{tag}", st, _assemble_copies, after)))

    P = dict(b_gate=b_gate, rel_bias=rel_bias, conv_dw_b=conv_dw_b, conv_ln_g=conv_ln_g, conv_ln_b=conv_ln_b,
             norm_mix_pre=norm_mix_pre + launched, norm_mix_post=norm_mix_post, norm_ffn_pre=norm_ffn_pre,
             norm_ffn_post=norm_ffn_post, ffn_conv_b=ffn_conv_b)
    loss_tile, grad_x, G = _local_step(xs, target, P, late_weights, on_grad)

    small = [k for k in names if k not in big]
    packed = _pack([loss_tile[:1]] + [G[k] for k in small])
    (allsum,) = _split_start("small_grad_allsum_start",
                             [([packed], [jnp.zeros((N_DEVICES,) + packed.shape, F32)], N_DEVICES - 1, _allsum_copies)], core_arr)

    reduced, grads, deltas, new_m, new_v = {}, {}, {}, {}, {}

    def update(keys):
        for k in keys:
            gk, d, mn, vn = _adamw_call(weights[k][0], reduced[k], m_in[k][0], v_in[k][0], f"adamw_{k}")
            grads[k], deltas[k], new_m[k], new_v[k] = gk[None], d[None], mn[None], vn[None]

    reduced.update(assemble_wait(others, assembling["others"], [allsum["tile"], grad_x], "others"))
    update(others)
    assembling["w_in"] = assemble_start(["w_in"], [deltas[k] for k in others], "w_in")

    me_chip = jnp.stack([4 * xi + 2 * yi + ci, chip]).astype(jnp.int32)
    mine, landed = _split_wait("small_grad_allsum_wait", allsum, _allsum_copies, assembling["w_in"]["tile"])
    piece_shapes = [(1, LANES)] + [(G[k].size // LANES, LANES) if k == "rel_bias" else G[k].shape for k in small]
    piece_cols = [0] + [weights[k].shape[2] if k in ("conv_dw_w", "ffn_conv_w") else 0 for k in small]
    loss_row, *summed = _ordered_sum_call(mine, landed, me_chip, piece_shapes, piece_cols)
    loss = loss_row[0, 0]
    for k, gsum in zip(small, summed):
        grads[k] = gsum.reshape(weights[k].shape)
    ds, mns, vns = _adamw_small_call([weights[k] for k in small], [grads[k] for k in small],
                                     [m_in[k] for k in small], [v_in[k] for k in small])
    deltas.update(zip(small, ds))
    new_m.update(zip(small, mns))
    new_v.update(zip(small, vns))
    reduced.update(assemble_wait(["w_in"], assembling["w_in"], list(ds), "w_in"))
    update(["w_in"])

    return (loss, grad_x[None], *[grads[k] for k in names], *[deltas[k] for k in names],
            *[new_m[k] for k in names], *[new_v[k] for k in names])
```

```python
import functools
import math

import jax
import jax.numpy as jnp
import numpy as np
from jax import lax
from jax.experimental import pallas as pl
from jax.experimental.pallas import tpu as pltpu

F32 = jnp.float32
BF16 = jnp.bfloat16
MESH = pl.DeviceIdType.MESH

HEAD_DIM = 128
HEADS_PER_GROUP = 4
DILATED_PATTERNS = ((128, 1), (512, 4), (2048, 16))
N_GROUPS = 3
N_HEADS = N_GROUPS * HEADS_PER_GROUP
SPAN = 128
GROUP_WIDTH = HEADS_PER_GROUP * HEAD_DIM
CONV_WIDTH = 31
FFN_CONV_WIDTH = 3
N_BUCKETS = 32
MAX_DISTANCE = 2048
RMS_EPS = 1e-6
LN_EPS = 1e-5
NEG_INF = -1e30
ADAM_LR = 0.001
ADAM_B1 = 0.9
ADAM_B2 = 0.999
ADAM_EPS = 1e-08
ADAM_WD = 0.01
ADAM_STEP = 10

LANES = 128
SUBLANES = 8
PACKED_ROWS = 16
ROW_TILE = 512
GATE_ROWS, GATE_COLS = 512, 512
TIME_BLOCK = 128
CONV_PAD = 32
FFN_PAD = 8
VMEM_LIMIT = 56 << 20


def _params(sem=None, vmem=None):
    kw = {}
    if sem is not None:
        kw["dimension_semantics"] = sem
    if vmem is not None:
        kw["vmem_limit_bytes"] = vmem
    return pltpu.CompilerParams(**kw)


def _pick(n, cands):
    for c in cands:
        if n % c == 0:
            return c
    return n


ELEMENTWISE_TILE_BYTES = 3 << 19


def _row_tile(rows, cols):
    for align in (16, SUBLANES):
        fits = [t for t in range(align, rows + 1, align) if rows % t == 0 and t * cols * 4 <= ELEMENTWISE_TILE_BYTES]
        if fits:
            return max(fits)
    return SUBLANES


N_CHIPS = 4
M_TILES = (1024, 1408, 512, 256, 128)
N_TILES = (1024, 512, 1408, 256, 128)
K_TILES = (2176, 2048, 1408, 1024, 512, 256, 128)


def _matmul(a, b, mode, name, out_shards=False, tm=None):
    assert a.dtype == BF16 and b.dtype == BF16, (name, a.dtype, b.dtype)
    b3 = b.ndim == 3
    tn = tk = None
    halves = None
    if mode == "nn":
        M, K = a.shape
        N = b.shape[-1] * (N_CHIPS if b3 else 1)
        tn = b.shape[-1] if b3 else None
    elif mode == "nt":
        if a.ndim == 3:
            halves = a.shape[2]
        M, K = a.shape[-2], a.shape[-1] * (a.shape[0] if a.ndim == 3 else 1)
        N = b.shape[-2]
        tk = b.shape[-1] if b3 else None
    else:
        if b3:
            halves = b.shape[2]
        K, M = a.shape
        N = b.shape[-1] * (b.shape[0] if b3 else 1)
        tn = N // N_CHIPS if out_shards else None
    tm = tm or _pick(M, M_TILES)
    tn = tn or _pick(N, N_TILES)
    tk = tk or _pick(K, K_TILES)
    nk = K // tk
    dn = {"nn": (((1,), (0,)), ((), ())), "nt": (((1,), (1,)), ((), ())), "tn": (((0,), (0,)), ((), ()))}[mode]

    def body(a_ref, b_ref, o_ref):
        if nk == 1:
            o_ref[...] = lax.dot_general(a_ref[...], b_ref[...], dn, preferred_element_type=F32)
        else:
            @pl.when(pl.program_id(2) == 0)
            def _():
                o_ref[...] = jnp.zeros_like(o_ref)

            o_ref[...] += lax.dot_general(a_ref[...], b_ref[...], dn, preferred_element_type=F32)

    if mode == "tn":
        a_spec = pl.BlockSpec((tk, tm), lambda i, j, k: (k, i))
    elif halves:
        per = halves // tk
        a_spec = pl.BlockSpec((None, tm, tk), lambda i, j, k: (k // per, i, k % per))
    else:
        a_spec = pl.BlockSpec((tm, tk), lambda i, j, k: (i, k))
    if mode == "nn":
        b_spec = pl.BlockSpec((None, tk, tn), lambda i, j, k: (j, k, 0)) if b3 else pl.BlockSpec((tk, tn), lambda i, j, k: (k, j))
    elif mode == "nt":
        b_spec = pl.BlockSpec((None, tn, tk), lambda i, j, k: (k, j, 0)) if b3 else pl.BlockSpec((tn, tk), lambda i, j, k: (j, k))
    elif halves:
        per = halves // tn
        b_spec = pl.BlockSpec((None, tk, tn), lambda i, j, k: (j // per, k, j % per))
    else:
        b_spec = pl.BlockSpec((tk, tn), lambda i, j, k: (k, j))
    if out_shards:
        out_spec = pl.BlockSpec((None, tm, tn), lambda i, j, k: (j, i, 0))
        out_shape = jax.ShapeDtypeStruct((N_CHIPS, M, tn), F32)
    else:
        out_spec = pl.BlockSpec((tm, tn), lambda i, j, k: (i, j))
        out_shape = jax.ShapeDtypeStruct((M, N), F32)
    return pl.pallas_call(
        body, name=name, grid=(M // tm, N // tn, nk),
        in_specs=[a_spec, b_spec], out_specs=out_spec, out_shape=out_shape,
        compiler_params=_params(("parallel", "parallel", "arbitrary"), VMEM_LIMIT),
    )(a, b)


def _grad_half_matmul(a, name, chip_half, b, init=None):
    K, M = a.shape
    parts = b.ndim == 3
    N = b.shape[-1] * (b.shape[0] if parts else 1)
    h, tn = M // 2, N // N_CHIPS
    summed = init is not None
    dn = (((0,), (0,)), ((), ()))

    def body(ch_ref, a_ref, b_ref, *rest):
        product = lax.dot_general(a_ref[...], b_ref[...], dn, preferred_element_type=F32)
        if not summed:
            rest[0][...] = product
            return
        init_ref, own_ref, sum16_ref = rest
        total = product + init_ref[...]
        sum16_ref[...] = total.astype(BF16)

        @pl.when(pl.program_id(0) == ch_ref[0])
        def _():
            own_ref[...] = total

    if parts:
        per = b.shape[2] // tn
        b_spec = pl.BlockSpec((None, K, tn), lambda j, ch_ref: (j // per, 0, j % per))
    else:
        b_spec = pl.BlockSpec((K, tn), lambda j, ch_ref: (0, j))
    shard_spec = pl.BlockSpec((None, h, tn), lambda j, ch_ref: (j, 0, 0))
    own_spec = pl.BlockSpec((h, tn), lambda j, ch_ref: (0, 0))
    shape = (N_CHIPS, h, tn)
    return pl.pallas_call(
        body, name=name + ("_mine" if summed else "_theirs"),
        grid_spec=pltpu.PrefetchScalarGridSpec(
            num_scalar_prefetch=1, grid=(N_CHIPS,),
            in_specs=[pl.BlockSpec((K, h), lambda j, ch_ref: (0, ch_ref[1])), b_spec] + [shard_spec] * summed,
            out_specs=[own_spec, shard_spec] if summed else shard_spec),
        out_shape=[jax.ShapeDtypeStruct((h, tn), F32), jax.ShapeDtypeStruct(shape, BF16)] if summed
        else jax.ShapeDtypeStruct(shape, F32),
        compiler_params=_params(("arbitrary",), VMEM_LIMIT),
    )(chip_half, a, b, *([init] if summed else []))


def _rms(x, g):
    r = lax.rsqrt(jnp.mean(x * x, axis=-1, keepdims=True) + RMS_EPS)
    return x * r * g


def _rms_bwd(x, g, dy):
    r = lax.rsqrt(jnp.mean(x * x, axis=-1, keepdims=True) + RMS_EPS)
    n = x * r
    dn = dy * g
    dx = r * (dn - n * jnp.mean(dn * n, axis=-1, keepdims=True))
    return dx, jnp.sum(dy * n, axis=0, keepdims=True)


def _sigmoid(x):
    return 1.0 / (1.0 + jnp.exp(-x))


_GELU_C = math.sqrt(2.0 / math.pi)


def _gelu(x):
    return 0.5 * x * (1.0 + jnp.tanh(_GELU_C * (x + 0.044715 * x * x * x)))


def _gelu_and_grad(x):
    x2 = x * x
    t = jnp.tanh(_GELU_C * x * (1.0 + 0.044715 * x2))
    half = 0.5 * (1.0 + t)
    return x * half, half + (0.5 * _GELU_C) * x * (1.0 - t * t) * (1.0 + (3.0 * 0.044715) * x2)


def _row_spec(width, col_block=0):
    return pl.BlockSpec((ROW_TILE, width), lambda i: (i, col_block))


def _vec_spec(width, col_block=0):
    return pl.BlockSpec((1, width), lambda i: (0, col_block))


def _accumulate(ref, part):
    @pl.when(pl.program_id(0) == 0)
    def _():
        ref[...] = part

    @pl.when(pl.program_id(0) > 0)
    def _():
        ref[...] += part


def _rms_fwd_call(x, g):
    S, D = x.shape

    def body(x_ref, g_ref, h_ref):
        h_ref[...] = _rms(x_ref[...], g_ref[...]).astype(BF16)

    return pl.pallas_call(
        body, name="rms_mix_pre", grid=(S // ROW_TILE,),
        in_specs=[_row_spec(D), _vec_spec(D)], out_specs=_row_spec(D),
        out_shape=jax.ShapeDtypeStruct((S, D), BF16),
        compiler_params=_params(("parallel",)),
    )(x, g)


def _ln_silu_call(c1, g, b):
    S, C = c1.shape

    def body(c_ref, g_ref, b_ref, o_ref):
        xv = c_ref[...]
        mu = jnp.mean(xv, axis=-1, keepdims=True)
        xc = xv - mu
        var = jnp.mean(xc * xc, axis=-1, keepdims=True)
        z = xc * lax.rsqrt(var + LN_EPS) * g_ref[...] + b_ref[...]
        o_ref[...] = (z * _sigmoid(z)).astype(BF16)

    return pl.pallas_call(
        body, name="conv_ln_silu", grid=(S // ROW_TILE,),
        in_specs=[_row_spec(C), _vec_spec(C), _vec_spec(C)], out_specs=_row_spec(C),
        out_shape=jax.ShapeDtypeStruct((S, C), BF16),
        compiler_params=_params(("parallel",)),
    )(c1, g, b)


def _ln_silu_bwd_call(c1, g, b, dy, w):
    S, C = c1.shape
    N = dy.shape[1]

    def body(c_ref, g_ref, b_ref, dy_ref, w_ref, dx_ref, dg_ref, db_ref):
        dc = lax.dot_general(dy_ref[...], w_ref[...], (((1,), (1,)), ((), ())), preferred_element_type=F32)
        xv = c_ref[...]
        mu = jnp.mean(xv, axis=-1, keepdims=True)
        xc = xv - mu
        rs = lax.rsqrt(jnp.mean(xc * xc, axis=-1, keepdims=True) + LN_EPS)
        xh = xc * rs
        z = xh * g_ref[...] + b_ref[...]
        sg = _sigmoid(z)
        dz = dc * (sg * (1.0 + z * (1.0 - sg)))
        dxh = dz * g_ref[...]
        dx_ref[...] = rs * (dxh - jnp.mean(dxh, axis=-1, keepdims=True) - xh * jnp.mean(dxh * xh, axis=-1, keepdims=True))
        _accumulate(dg_ref, jnp.sum(dz * xh, axis=0, keepdims=True))
        _accumulate(db_ref, jnp.sum(dz, axis=0, keepdims=True))

    return pl.pallas_call(
        body, name="conv_pw_dx_ln_silu_bwd", grid=(S // ROW_TILE,),
        in_specs=[_row_spec(C), _vec_spec(C), _vec_spec(C), _row_spec(N), pl.BlockSpec((C, N), lambda i: (0, 0))],
        out_specs=[_row_spec(C), _vec_spec(C), _vec_spec(C)],
        out_shape=[jax.ShapeDtypeStruct((S, C), F32), jax.ShapeDtypeStruct((1, C), F32), jax.ShapeDtypeStruct((1, C), F32)],
        compiler_params=_params(("arbitrary",), VMEM_LIMIT),
    )(c1, g, b, dy, w)


def _mix_call(proj, gate_col0, b_gate, y_a, y_c):
    S, D = y_a.shape
    w = GATE_COLS
    nc = D // w
    ga0, gc0 = gate_col0 // w, (gate_col0 + D) // w

    def body(ga_ref, gc_ref, ba_ref, bc_ref, ya_ref, yc_ref, o_ref):
        o_ref[...] = (_sigmoid(ga_ref[...] + ba_ref[...]) * ya_ref[...]
                      + _sigmoid(gc_ref[...] + bc_ref[...]) * yc_ref[...]).astype(BF16)

    tile = lambda off: pl.BlockSpec((GATE_ROWS, w), lambda i, j: (i, off + j))
    vec = lambda off: pl.BlockSpec((1, w), lambda i, j: (0, off + j))
    return pl.pallas_call(
        body, name="gate_mix", grid=(S // GATE_ROWS, nc),
        in_specs=[tile(ga0), tile(gc0), vec(0), vec(nc), tile(0), tile(0)],
        out_specs=tile(0), out_shape=jax.ShapeDtypeStruct((S, D), BF16),
        compiler_params=_params(("parallel", "parallel")),
    )(proj, proj, b_gate, b_gate, y_a, y_c)


def _window_stores(stage_ref, slot, dst_ref, rows, cols, sems):
    width = stage_ref.shape[-1]
    return [pltpu.make_async_copy(stage_ref.at[slot, p], dst_ref.at[rows, pl.ds(pl.multiple_of(c, LANES), width)], sems.at[slot, p])
            for p, c in enumerate(cols)]


def _staged_window_stores(stage_ref, dst_ref, sems, step, n_steps, rows, cols, fill):
    slot = step % 2
    copies = lambda s: _window_stores(stage_ref, s, dst_ref, rows, cols, sems)

    @pl.when(step >= 2)
    def _():
        for cp in copies(slot):
            cp.wait()

    fill(slot)
    for cp in copies(slot):
        cp.start()

    @pl.when(step == n_steps - 1)
    def _():
        for s in ([slot, 1 - slot] if n_steps > 1 else [slot]):
            for cp in copies(s):
                cp.wait()


def _mix_bwd_call(dout, w_out, proj, gate_col0, b_gate, y_a, y_c):
    S, D = y_a.shape
    w = GATE_COLS
    nc = D // w
    nr = S // GATE_ROWS
    ga0, gc0 = gate_col0 // w, (gate_col0 + D) // w

    def body(do_ref, w_ref, ga_ref, gc_ref, ba_ref, bc_ref, ya_ref, yc_ref, dya_ref, dyc_ref, dproj_ref, dba_ref, dbc_ref,
             stage_ref, sems):
        j, i = pl.program_id(0), pl.program_id(1)
        dm = lax.dot_general(do_ref[...], w_ref[...], (((1,), (1,)), ((), ())), preferred_element_type=F32)
        sa = _sigmoid(ga_ref[...] + ba_ref[...])
        sc = _sigmoid(gc_ref[...] + bc_ref[...])
        dya_ref[...] = (dm * sa).astype(BF16)
        dyc_ref[...] = (dm * sc).astype(BF16)
        dga = dm * ya_ref[...] * sa * (1.0 - sa)
        dgc = dm * yc_ref[...] * sc * (1.0 - sc)

        def fill(slot):
            stage_ref[slot, 0] = dga.astype(BF16)
            stage_ref[slot, 1] = dgc.astype(BF16)

        rows = pl.ds(pl.multiple_of(i * GATE_ROWS, GATE_ROWS), GATE_ROWS)
        _staged_window_stores(stage_ref, dproj_ref, sems, j * nr + i, nc * nr, rows,
                              [gate_col0 + j * w, gate_col0 + D + j * w], fill)
        pa = jnp.sum(dga, axis=0, keepdims=True)
        pc = jnp.sum(dgc, axis=0, keepdims=True)

        @pl.when(i == 0)
        def _():
            dba_ref[...] = pa
            dbc_ref[...] = pc

        @pl.when(i > 0)
        def _():
            dba_ref[...] += pa
            dbc_ref[...] += pc

    tile = lambda off: pl.BlockSpec((GATE_ROWS, w), lambda j, i: (i, off + j))
    vec = lambda off: pl.BlockSpec((1, w), lambda j, i: (0, off + j))
    return pl.pallas_call(
        body, name="gate_mix_bwd", grid=(nc, nr),
        in_specs=[pl.BlockSpec((GATE_ROWS, dout.shape[1]), lambda j, i: (i, 0)), pl.BlockSpec((w, w_out.shape[1]), lambda j, i: (j, 0)),
                  tile(ga0), tile(gc0), vec(0), vec(nc), tile(0), tile(0)],
        out_specs=[tile(0), tile(0), ANY, vec(0), vec(0)],
        out_shape=[jax.ShapeDtypeStruct((S, D), BF16)] * 2 + [jax.ShapeDtypeStruct((S, proj.shape[1]), BF16)] + [
                   jax.ShapeDtypeStruct((1, D), F32), jax.ShapeDtypeStruct((1, D), F32)],
        scratch_shapes=[pltpu.VMEM((2, 2, GATE_ROWS, w), BF16), pltpu.SemaphoreType.DMA((2, 2))],
        compiler_params=_params(("arbitrary", "arbitrary")),
    )(dout, w_out, proj, proj, b_gate, b_gate, y_a, y_c)


def _res1_call(x, out, g_post, g_pre):
    S, D = x.shape

    def body(x_ref, o_ref, gp_ref, gq_ref, x1_ref, h2_ref):
        x1 = x_ref[...] + _rms(o_ref[...], gp_ref[...])
        x1_ref[...] = x1
        h2_ref[...] = _rms(x1, gq_ref[...]).astype(BF16)

    return pl.pallas_call(
        body, name="residual_mix", grid=(S // ROW_TILE,),
        in_specs=[_row_spec(D), _row_spec(D), _vec_spec(D), _vec_spec(D)],
        out_specs=[_row_spec(D), _row_spec(D)],
        out_shape=[jax.ShapeDtypeStruct((S, D), F32), jax.ShapeDtypeStruct((S, D), BF16)],
        compiler_params=_params(("parallel",)),
    )(x, out, g_post, g_pre)


def _ffn_down_loss_call(f, w_down, x1, g_post, target):
    S, K = f.shape
    D = w_down.shape[1]
    tk = _pick(K, K_TILES)
    nk = K // tk

    def body(f_ref, w_ref, x1_ref, g_ref, t_ref, loss_ref, dx_ref, dy_ref, dg_ref, y_ref):
        k = pl.program_id(1)
        product = lax.dot_general(f_ref[...], w_ref[...], (((1,), (0,)), ((), ())), preferred_element_type=F32)

        @pl.when(k == 0)
        def _():
            y_ref[...] = product

        @pl.when(k > 0)
        def _():
            y_ref[...] += product

        @pl.when(k == nk - 1)
        def _():
            yv, gv = y_ref[...], g_ref[...]
            err = x1_ref[...] + _rms(yv, gv) - t_ref[...]
            dx2 = err * (1.0 / D)
            dx_ref[...] = dx2
            dy, dg = _rms_bwd(yv, gv, dx2)
            dy_ref[...] = dy.astype(BF16)
            _accumulate(dg_ref, dg)
            part = 0.5 * jnp.sum(jnp.mean(err * err, axis=-1, keepdims=True), axis=0, keepdims=True)
            _accumulate(loss_ref, jnp.broadcast_to(part, (SUBLANES, LANES)))

    rows = pl.BlockSpec((ROW_TILE, D), lambda i, k: (i, 0))
    vec = pl.BlockSpec((1, D), lambda i, k: (0, 0))
    return pl.pallas_call(
        body, name="ffn_down_loss", grid=(S // ROW_TILE, nk),
        in_specs=[pl.BlockSpec((ROW_TILE, tk), lambda i, k: (i, k)), pl.BlockSpec((tk, D), lambda i, k: (k, 0)), rows, vec, rows],
        out_specs=[pl.BlockSpec((SUBLANES, LANES), lambda i, k: (0, 0)), rows, rows, vec],
        out_shape=[jax.ShapeDtypeStruct((SUBLANES, LANES), F32), jax.ShapeDtypeStruct((S, D), F32),
                   jax.ShapeDtypeStruct((S, D), BF16), jax.ShapeDtypeStruct((1, D), F32)],
        scratch_shapes=[pltpu.VMEM((ROW_TILE, D), F32)],
        compiler_params=_params(("arbitrary", "arbitrary"), VMEM_LIMIT),
    )(f, w_down, x1, g_post, target)


def _mid_bwd_call(x1, g_pre, dh2, dx2, out, g_post):
    S, D = x1.shape

    def body(x1_ref, gq_ref, dh_ref, dx2_ref, o_ref, gp_ref, dx1_ref, do_ref, dgq_ref, dgp_ref):
        d, dgq = _rms_bwd(x1_ref[...], gq_ref[...], dh_ref[...])
        dx1 = dx2_ref[...] + d
        dx1_ref[...] = dx1
        do, dgp = _rms_bwd(o_ref[...], gp_ref[...], dx1)
        do_ref[...] = do.astype(BF16)
        _accumulate(dgq_ref, dgq)
        _accumulate(dgp_ref, dgp)

    return pl.pallas_call(
        body, name="residual_mix_bwd", grid=(S // ROW_TILE,),
        in_specs=[_row_spec(D), _vec_spec(D), _row_spec(D), _row_spec(D), _row_spec(D), _vec_spec(D)],
        out_specs=[_row_spec(D), _row_spec(D), _vec_spec(D), _vec_spec(D)],
        out_shape=[jax.ShapeDtypeStruct((S, D), F32), jax.ShapeDtypeStruct((S, D), BF16)] + [jax.ShapeDtypeStruct((1, D), F32)] * 2,
        compiler_params=_params(("arbitrary",)),
    )(x1, g_pre, dh2, dx2, out, g_post)


def _in_bwd_call(x, g, dh1, dx1):
    S, D = x.shape

    def body(x_ref, g_ref, dh_ref, dx1_ref, gx_ref, dg_ref):
        d, dg = _rms_bwd(x_ref[...], g_ref[...], dh_ref[...])
        gx_ref[...] = dx1_ref[...] + d
        _accumulate(dg_ref, dg)

    return pl.pallas_call(
        body, name="rms_mix_pre_bwd", grid=(S // ROW_TILE,),
        in_specs=[_row_spec(D), _vec_spec(D), _row_spec(D), _row_spec(D)],
        out_specs=[_row_spec(D), _vec_spec(D)],
        out_shape=[jax.ShapeDtypeStruct((S, D), F32), jax.ShapeDtypeStruct((1, D), F32)],
        compiler_params=_params(("arbitrary",)),
    )(x, g, dh1, dx1)


def _bucket_table(dilation):
    qi = np.arange(SPAN)[:, None]
    ki = np.arange(2 * SPAN)[None, :]
    dist = np.maximum(qi + SPAN - ki, 0) * dilation
    max_exact = N_BUCKETS // 2
    d = np.maximum(dist, 1).astype(np.float64)
    large = max_exact + (np.log(d / max_exact) / math.log(MAX_DISTANCE / max_exact) * (N_BUCKETS - max_exact)).astype(np.int32)
    large = np.minimum(large, N_BUCKETS - 1)
    return np.where(dist < max_exact, dist, large).astype(np.int32)


def _bucket_tables():
    return jnp.asarray(np.stack([_bucket_table(r) for _, r in DILATED_PATTERNS]))


def _bias_table_call(rel_bias, buckets):
    def body(rb_ref, bk_ref, o_ref):
        for h in range(N_HEADS):
            bk = bk_ref[h // HEADS_PER_GROUP]

            def step(b, acc):
                return jnp.where(bk == b, rb_ref[b, h], acc)

            o_ref[h] = lax.fori_loop(0, N_BUCKETS, step, jnp.zeros((SPAN, 2 * SPAN), F32))

    return pl.pallas_call(
        body, name="rel_bias_table",
        in_specs=[pl.BlockSpec(memory_space=pltpu.SMEM), pl.BlockSpec(memory_space=pltpu.VMEM)],
        out_specs=pl.BlockSpec(memory_space=pltpu.VMEM),
        out_shape=jax.ShapeDtypeStruct((N_HEADS, SPAN, 2 * SPAN), F32),
    )(rel_bias, buckets)


def _bias_grad_call(dbias, buckets):
    def body(db_ref, bk_ref, o_ref, rows_ref):
        for h in range(N_HEADS):
            bk = bk_ref[h // HEADS_PER_GROUP]
            dv = db_ref[h]

            def step(b, carry):
                rows_ref[h, b] = jnp.sum(jnp.where(bk == b, dv, 0.0), axis=0, keepdims=True)
                return carry

            lax.fori_loop(0, N_BUCKETS, step, 0)
        o_ref[...] = jnp.sum(rows_ref[...], axis=-1, keepdims=True)

    out = pl.pallas_call(
        body, name="rel_bias_grad",
        in_specs=[pl.BlockSpec(memory_space=pltpu.VMEM), pl.BlockSpec(memory_space=pltpu.VMEM)],
        out_specs=pl.BlockSpec(memory_space=pltpu.VMEM),
        out_shape=jax.ShapeDtypeStruct((N_HEADS, N_BUCKETS, 1, 1), F32),
        scratch_shapes=[pltpu.VMEM((N_HEADS, N_BUCKETS, 1, 2 * SPAN), F32)],
    )(dbias, buckets)
    return out.reshape(N_HEADS, N_BUCKETS).T


def _dot_nt(a, b):
    return lax.dot_general(a, b, (((1,), (1,)), ((), ())), preferred_element_type=F32)


def _dot_nn(a, b):
    return lax.dot_general(a, b, (((1,), (0,)), ((), ())), preferred_element_type=F32)


def _dot_tn(a, b):
    return lax.dot_general(a, b, (((0,), (0,)), ((), ())), preferred_element_type=F32)


def _band_masks(n, nb):
    qi = lax.broadcasted_iota(jnp.int32, (SPAN, SPAN), 0)
    ki = lax.broadcasted_iota(jnp.int32, (SPAN, SPAN), 1)
    prev_ok = jnp.logical_and(ki >= qi, n > 0)
    cur_ok = ki <= qi
    next_ok = jnp.logical_and(ki >= qi, n < nb - 1)
    return prev_ok, cur_ok, next_ok


def _wide_band_mask(n):
    qi = lax.broadcasted_iota(jnp.int32, (SPAN, 2 * SPAN), 0)
    ki = lax.broadcasted_iota(jnp.int32, (SPAN, 2 * SPAN), 1)
    prev_ok = jnp.logical_and(jnp.logical_and(ki < SPAN, ki >= qi), n > 0)
    cur_ok = jnp.logical_and(ki >= SPAN, ki - SPAN <= qi)
    return jnp.logical_or(prev_ok, cur_ok)


def _attn_plan(S, group):
    r = DILATED_PATTERNS[group][1]
    hp, per = (HEADS_PER_GROUP, 1) if r == 1 else (2, 4)
    return r, S // (r * SPAN), hp, per


def _residue_rows(rho, r):
    return slice(None) if r == 1 else pl.ds(rho, SPAN, stride=r)


def _for_residues(r, per, fn):
    if r == per:
        for u in range(per):
            fn(u)
        return

    def step(i, carry):
        for u in range(per):
            fn(i * per + u)
        return carry

    lax.fori_loop(0, r // per, step, 0)


def _attn_fwd_call(proj, bias, group, merge_with=None):
    S = proj.shape[0]
    r, nb, hp, per = _attn_plan(S, group)
    scale = HEAD_DIM ** -0.5
    kinds = ("q", "kp", "kc", "vp", "vc") if nb > 1 else ("q", "kc", "vc")
    merge = merge_with is not None
    assert not merge or (r == 1 and hp == HEADS_PER_GROUP and len(merge_with) == 4)

    per_kind = _refs_per_kind(r, hp)

    def body(*refs):
        ins = {kind: refs[i * per_kind:(i + 1) * per_kind] for i, kind in enumerate(kinds)}
        b_ref, *rest = refs[len(kinds) * per_kind:]
        if merge:
            o2_ref, s2_ref, o3_ref, s3_ref, a_ref, ab_ref, lse_ref = rest
        else:
            o_ref, lse_ref = rest
        n = pl.program_id(1)
        prev_ok, cur_ok, _ = _band_masks(n, nb)

        band_ok = _wide_band_mask(n) if nb > 1 else cur_ok

        def residue(rho):
            rows = _residue_rows(rho, r)
            for j in range(hp):
                get = lambda kind: _head_rows(ins[kind], j, rows, r).astype(BF16)
                q = get("q")
                if nb > 1:
                    keys, vals, bias_j = jnp.concatenate([get("kp"), get("kc")], axis=0), jnp.concatenate([get("vp"), get("vc")], axis=0), b_ref[j]
                else:
                    keys, vals, bias_j = get("kc"), get("vc"), b_ref[j, :, SPAN:]
                s = jnp.where(band_ok, _dot_nt(q, keys) * scale + bias_j, NEG_INF)
                m = jnp.max(s, axis=-1, keepdims=True)
                p = jnp.exp(s - m)
                den = jnp.sum(p, axis=-1, keepdims=True)
                o1 = _dot_nn(p.astype(BF16), vals) / den
                s1 = jnp.broadcast_to(m + jnp.log(den), (SPAN, HEAD_DIM))
                if not merge:
                    o_ref[j, rows, :] = o1
                    lse_ref[j, rows, :] = s1
                    continue
                sl = slice(j * HEAD_DIM, (j + 1) * HEAD_DIM)
                s2, s3 = s2_ref[j], s3_ref[j]
                mx = jnp.maximum(jnp.maximum(s1, s2), s3)
                w1 = jnp.exp(s1 - mx)
                w2 = jnp.exp(s2 - mx)
                w3 = jnp.exp(s3 - mx)
                total = w1 + w2 + w3
                merged = (w1 * o1 + w2 * o2_ref[j] + w3 * o3_ref[j]) / total
                a_ref[:, sl] = merged
                ab_ref[:, sl] = merged.astype(BF16)
                lse_ref[:, sl] = mx + jnp.log(total)

        _for_residues(r, per, residue)

    in_specs = [_head_spec(r, nb, hp, kind, group, jj) for kind in kinds for jj in range(per_kind)]
    in_specs.append(pl.BlockSpec((hp, SPAN, 2 * SPAN), lambda j, n: (group * (HEADS_PER_GROUP // hp) + j, 0, 0)))
    out = pl.BlockSpec((hp, r * SPAN, HEAD_DIM), lambda j, n: (j, n, 0))
    operands = [proj] * (len(in_specs) - 1) + [bias]
    if merge:
        in_specs += [out] * 4
        operands += list(merge_with)
        rows = pl.BlockSpec((SPAN, GROUP_WIDTH), lambda j, n: (n, 0))
        out_specs = [rows] * 3
        out_shape = [jax.ShapeDtypeStruct((S, GROUP_WIDTH), dt) for dt in (F32, BF16, F32)]
    else:
        out_specs = [out] * 2
        out_shape = [jax.ShapeDtypeStruct((HEADS_PER_GROUP, S, HEAD_DIM), F32)] * 2
    return pl.pallas_call(
        body, name=f"attn_fwd_g{group}", grid=(HEADS_PER_GROUP // hp, nb),
        in_specs=in_specs, out_specs=out_specs, out_shape=out_shape,
        compiler_params=_params(("parallel", "parallel"), VMEM_LIMIT),
    )(*operands)


_PROJ_PART = dict(q=0, qn=0, kp=1, kc=1, vp=2, vc=2)


def _refs_per_kind(r, hp):
    return 1 if r == 1 else hp


def _head_rows(refs, j, rows, r):
    return refs[0][:, j * HEAD_DIM:(j + 1) * HEAD_DIM] if r == 1 else refs[j][rows, :]


def _head_spec(r, nb, hp, kind, group, jj):
    if kind in _PROJ_PART:
        base = (_PROJ_PART[kind] * N_GROUPS + group) * HEADS_PER_GROUP
    else:
        base = 0
    if kind.endswith("p"):
        row = lambda n: jnp.maximum(n - 1, 0)
    elif kind.endswith("n"):
        row = lambda n: jnp.minimum(n + 1, nb - 1)
    else:
        row = lambda n: n
    if r == 1:
        return pl.BlockSpec((SPAN, hp * HEAD_DIM), lambda j, n: (row(n), base // hp + j))
    return pl.BlockSpec((r * SPAN, HEAD_DIM), lambda j, n: (row(n), base + j * hp + jj))


def _attn_bwd_call(proj, bias, a, da, lse, group, dproj):
    S = proj.shape[0]
    r, nb, hp, per = _attn_plan(S, group)
    scale = HEAD_DIM ** -0.5
    kinds = ("q", "qn", "kp", "kc", "vp", "vc", "da", "dan", "lse", "lsen", "a", "an") if nb > 1 else ("q", "kc", "vc", "da", "lse", "a")
    source = dict(da=da, dan=da, lse=lse, lsen=lse, a=a, an=a)

    per_kind = _refs_per_kind(r, hp)
    per_group = HEADS_PER_GROUP // hp
    block_rows = r * SPAN

    def body(*refs):
        ins = {kind: refs[i * per_kind:(i + 1) * per_kind] for i, kind in enumerate(kinds)}
        b_ref, _, dproj_ref, db_ref, stage_ref, sems = refs[len(kinds) * per_kind:][:6]
        strided_ref = None if r == 1 else refs[-1]
        jg, n = pl.program_id(0), pl.program_id(1)
        prev_ok, cur_ok, next_ok = _band_masks(n, nb)

        @pl.when(n == 0)
        def _():
            db_ref[...] = jnp.zeros_like(db_ref)

        band_ok = _wide_band_mask(n) if nb > 1 else cur_ok

        def fill(slot):
            def put(part, j, rows, value):
                if r == 1:
                    stage_ref[slot, part, :, j * HEAD_DIM:(j + 1) * HEAD_DIM] = value.astype(BF16)
                else:
                    strided_ref[part, j, rows, :] = value

            _for_residues(r, per, functools.partial(residue, put))
            if r > 1:
                for part in range(3):
                    for j in range(hp):
                        for t0 in range(0, block_rows, ROW_TILE):
                            stage_ref[slot, part, t0:t0 + ROW_TILE, j * HEAD_DIM:(j + 1) * HEAD_DIM] = (
                                strided_ref[part, j, t0:t0 + ROW_TILE, :].astype(BF16))

        def residue(put, rho):
            rows = _residue_rows(rho, r)
            for j in range(hp):
                get = lambda kind: _head_rows(ins[kind], j, rows, r)
                q = get("q").astype(BF16)
                kc = get("kc").astype(BF16)
                vc = get("vc").astype(BF16)
                da_q = get("da")
                dav = da_q.astype(BF16)
                lse_q = get("lse")
                dl_q = jnp.sum(get("a") * da_q, axis=-1, keepdims=True)
                if nb == 1:
                    pc = jnp.exp(jnp.where(cur_ok, _dot_nt(q, kc) * scale + b_ref[j, :, SPAN:], NEG_INF) - lse_q)
                    dsc = pc * (_dot_nt(dav, vc) - dl_q)
                    dsc_b = dsc.astype(BF16)
                    dq = _dot_nn(dsc_b, kc)
                    dk = _dot_tn(dsc_b, q)
                    dv = _dot_tn(pc.astype(BF16), dav)
                    db_ref[j, :, SPAN:] += dsc
                else:
                    qn = get("qn").astype(BF16)
                    da_n = get("dan")
                    dan = da_n.astype(BF16)
                    keys = jnp.concatenate([get("kp").astype(BF16), kc], axis=0)
                    vals = jnp.concatenate([get("vp").astype(BF16), vc], axis=0)
                    wide = lambda t: jnp.concatenate([t, t], axis=1)
                    p = jnp.exp(jnp.where(band_ok, _dot_nt(q, keys) * scale + b_ref[j], NEG_INF) - wide(lse_q))
                    ds = p * (_dot_nt(dav, vals) - dl_q)
                    dq = _dot_nn(ds.astype(BF16), keys)
                    db_ref[j] += ds
                    pn = jnp.exp(jnp.where(next_ok, _dot_nt(qn, kc) * scale + b_ref[j, :, :SPAN], NEG_INF) - get("lsen"))
                    dsn = pn * (_dot_nt(dan, vc) - jnp.sum(get("an") * da_n, axis=-1, keepdims=True))
                    both = lambda cur_part, next_part: jnp.concatenate([cur_part.astype(BF16), next_part.astype(BF16)], axis=0)
                    dk = _dot_tn(both(ds[:, SPAN:], dsn), jnp.concatenate([q, qn], axis=0))
                    dv = _dot_tn(both(p[:, SPAN:], pn), jnp.concatenate([dav, dan], axis=0))
                put(0, j, rows, dq * scale)
                put(1, j, rows, dk * scale)
                put(2, j, rows, dv)

        cols = [(part * N_GROUPS + group) * GROUP_WIDTH + jg * (hp * HEAD_DIM) for part in range(3)]
        rows = pl.ds(pl.multiple_of(n * block_rows, SPAN), block_rows)
        _staged_window_stores(stage_ref, dproj_ref, sems, jg * nb + n, per_group * nb, rows, cols, fill)

    band = (hp, SPAN, 2 * SPAN)
    in_specs = [_head_spec(r, nb, hp, kind, group, jj) for kind in kinds for jj in range(per_kind)]
    in_specs += [pl.BlockSpec(band, lambda j, n: (group * per_group + j, 0, 0)), ANY]
    operands = [source.get(kind, proj) for kind in kinds for _ in range(per_kind)] + [bias, dproj]
    scratch = [pltpu.VMEM((2, 3, block_rows, hp * HEAD_DIM), BF16), pltpu.SemaphoreType.DMA((2, 3))]
    if r > 1:
        scratch.append(pltpu.VMEM((3, hp, block_rows, HEAD_DIM), F32))
    return pl.pallas_call(
        body, name=f"attn_bwd_g{group}", grid=(per_group, nb),
        in_specs=in_specs,
        out_specs=[ANY, pl.BlockSpec(band, lambda j, n: (j, 0, 0))],
        out_shape=[jax.ShapeDtypeStruct(dproj.shape, BF16), jax.ShapeDtypeStruct((HEADS_PER_GROUP, SPAN, 2 * SPAN), F32)],
        input_output_aliases={len(operands) - 1: 0},
        scratch_shapes=scratch,
        compiler_params=_params(("arbitrary", "arbitrary"), VMEM_LIMIT),
    )(*operands)


def _tap_rows(xpad_ref, t0, k, width, pad):
    return xpad_ref[pl.ds(t0 + (pad - (width - 1 - k)), TIME_BLOCK), :]


def _conv_block(xpad_ref, t0, w_ref, width, pad):
    acc = None
    for k in range(width):
        term = w_ref[k:k + 1, :] * _tap_rows(xpad_ref, t0, k, width, pad)
        acc = term if acc is None else acc + term
    return acc


def _conv_transpose_block(dpad_ref, t0, w_ref, width):
    acc = None
    for k in range(width):
        term = w_ref[k:k + 1, :] * dpad_ref[pl.ds(t0 + (width - 1 - k), TIME_BLOCK), :]
        acc = term if acc is None else acc + term
    return acc


def _conv_weight_grad(xpad_ref, t0, dy, dw_ref, width, pad):
    for k in range(width):
        dw_ref[k:k + 1, :] += jnp.sum(dy * _tap_rows(xpad_ref, t0, k, width, pad), axis=0, keepdims=True)


def _time_loop(S, step, skip_first=0, skip_last=0):
    def it(tb, carry):
        step(pl.multiple_of(tb * TIME_BLOCK, TIME_BLOCK))
        return carry

    lax.fori_loop(skip_first, S // TIME_BLOCK - skip_last, it, 0)


def _fill_head(head_ref, x_ref, pad):
    head_ref[0:pad, :] = jnp.zeros((pad, LANES), F32)
    head_ref[pad:, :] = x_ref[0:TIME_BLOCK, :]


def _fill_tail(tail_ref, x_ref, pad):
    S = x_ref.shape[0]
    tail_ref[0:TIME_BLOCK, :] = x_ref[S - TIME_BLOCK:S, :]
    tail_ref[TIME_BLOCK:, :] = jnp.zeros((pad, LANES), F32)


def _conv_fwd_call(proj, col0, w, b):
    S = proj.shape[0]
    C = w.shape[1]
    nt = C // LANES
    v0, g0 = col0 // LANES, (col0 + C) // LANES

    def body(val_ref, gate_ref, w_ref, b_ref, o_ref, pad_ref):
        pad_ref[0:CONV_PAD, :] = jnp.zeros((CONV_PAD, LANES), F32)
        pad_ref[CONV_PAD:, :] = val_ref[...] * _sigmoid(gate_ref[...])

        def step(t0):
            o_ref[pl.ds(t0, TIME_BLOCK), :] = _conv_block(pad_ref, t0, w_ref, CONV_WIDTH, CONV_PAD) + b_ref[...]

        _time_loop(S, step)

    seq = lambda off: pl.BlockSpec((S, LANES), lambda i: (0, off + i))
    return pl.pallas_call(
        body, name="conv_module", grid=(nt,),
        in_specs=[seq(v0), seq(g0), pl.BlockSpec((CONV_WIDTH, LANES), lambda i: (0, i)), pl.BlockSpec((1, LANES), lambda i: (0, i))],
        out_specs=seq(0), out_shape=jax.ShapeDtypeStruct((S, C), F32),
        scratch_shapes=[pltpu.VMEM((S + CONV_PAD, LANES), F32)],
        compiler_params=_params(("parallel",)),
    )(proj, proj, w, b)


def _conv_bwd_call(proj, col0, w, dc1, dproj):
    S = proj.shape[0]
    C = w.shape[1]
    nt = C // LANES
    v0, g0 = col0 // LANES, (col0 + C) // LANES

    def body(val_ref, gate_ref, w_ref, dy_ref, _, dproj_ref, dw_ref, db_ref, xpad_ref, tail_ref, dwacc_ref, stage_ref, sems):
        i = pl.program_id(0)
        xpad_ref[0:CONV_PAD, :] = jnp.zeros((CONV_PAD, LANES), F32)
        xpad_ref[CONV_PAD:, :] = val_ref[...] * _sigmoid(gate_ref[...])
        _fill_tail(tail_ref, dy_ref, CONV_PAD)
        dwacc_ref[...] = jnp.zeros_like(dwacc_ref)

        def fill(slot):
            def block(t0, dy_src, dy_t0):
                rows = pl.ds(t0, TIME_BLOCK)
                _conv_weight_grad(xpad_ref, t0, dy_ref[rows, :], dwacc_ref, CONV_WIDTH, CONV_PAD)
                dc0 = _conv_transpose_block(dy_src, dy_t0, w_ref, CONV_WIDTH)
                sg = _sigmoid(gate_ref[rows, :])
                stage_ref[slot, 0, rows, :] = (dc0 * sg).astype(BF16)
                stage_ref[slot, 1, rows, :] = (dc0 * val_ref[rows, :] * sg * (1.0 - sg)).astype(BF16)

            _time_loop(S, lambda t0: block(t0, dy_ref, t0), skip_last=1)
            block(S - TIME_BLOCK, tail_ref, 0)

        _staged_window_stores(stage_ref, dproj_ref, sems, i, nt, pl.ds(0, S),
                              [col0 + i * LANES, col0 + C + i * LANES], fill)
        dw_ref[...] = dwacc_ref[...]
        db_ref[...] = jnp.sum(dy_ref[...], axis=0, keepdims=True)

    seq = lambda off: pl.BlockSpec((S, LANES), lambda i: (0, off + i))
    return pl.pallas_call(
        body, name="conv_module_bwd", grid=(nt,),
        in_specs=[seq(v0), seq(g0), pl.BlockSpec((CONV_WIDTH, LANES), lambda i: (0, i)), seq(0), ANY],
        out_specs=[ANY, pl.BlockSpec((CONV_PAD, LANES), lambda i: (0, i)), pl.BlockSpec((1, LANES), lambda i: (0, i))],
        out_shape=[jax.ShapeDtypeStruct(dproj.shape, BF16),
                   jax.ShapeDtypeStruct((CONV_PAD, C), F32), jax.ShapeDtypeStruct((1, C), F32)],
        input_output_aliases={4: 0},
        scratch_shapes=[pltpu.VMEM((S + CONV_PAD, LANES), F32), pltpu.VMEM((TIME_BLOCK + CONV_PAD, LANES), F32),
                        pltpu.VMEM((CONV_PAD, LANES), F32),
                        pltpu.VMEM((2, 2, S, LANES), BF16), pltpu.SemaphoreType.DMA((2, 2))],
        compiler_params=_params(("arbitrary",)),
    )(proj, proj, w, dc1, dproj)


def _ffn_fwd_call(u, w, b):
    S, C2 = u.shape
    C = C2 // 2
    width = _pick(C, (2 * LANES, LANES))
    nt = C // width

    def body(ug_ref, uv_ref, wg_ref, wv_ref, bg_ref, bv_ref, f_ref, xg_ref, xv_ref):
        zeros = jnp.zeros((FFN_PAD, LANES), F32)
        for part in range(width // LANES):
            lanes = slice(part * LANES, (part + 1) * LANES)
            xg_ref[0:FFN_PAD, :] = zeros
            xv_ref[0:FFN_PAD, :] = zeros
            xg_ref[FFN_PAD:, :] = ug_ref[:, lanes]
            xv_ref[FFN_PAD:, :] = uv_ref[:, lanes]

            def step(t0, lanes=lanes):
                cg = _conv_block(xg_ref, t0, wg_ref.at[:, lanes], FFN_CONV_WIDTH, FFN_PAD) + bg_ref[:, lanes]
                cv = _conv_block(xv_ref, t0, wv_ref.at[:, lanes], FFN_CONV_WIDTH, FFN_PAD) + bv_ref[:, lanes]
                f_ref[pl.ds(t0, TIME_BLOCK), lanes] = (_gelu(cg) * cv).astype(BF16)

            _time_loop(S, step)

    seq = lambda off: pl.BlockSpec((S, width), lambda i: (0, off + i))
    wsp = lambda off: pl.BlockSpec((FFN_CONV_WIDTH, width), lambda i: (0, off + i))
    bsp = lambda off: pl.BlockSpec((1, width), lambda i: (0, off + i))
    return pl.pallas_call(
        body, name="ffn_conv_geglu", grid=(nt,),
        in_specs=[seq(0), seq(nt), wsp(0), wsp(nt), bsp(0), bsp(nt)],
        out_specs=seq(0), out_shape=jax.ShapeDtypeStruct((S, C), BF16),
        scratch_shapes=[pltpu.VMEM((FFN_PAD + S, LANES), F32)] * 2,
        compiler_params=_params(("parallel",)),
    )(u, u, w, w, b, b)


def _ffn_bwd_call(u, w, b, df):
    S, C2 = u.shape
    C = C2 // 2
    nt = C // LANES

    def body(ug_ref, uv_ref, wg_ref, wv_ref, bg_ref, bv_ref, df_ref,
             du_ref, dwg_ref, dwv_ref, dbg_ref, dbv_ref,
             hg_ref, hv_ref, dg_ref, dv_ref, dwg_acc, dwv_acc, dbg_acc, dbv_acc):
        zeros = jnp.zeros((FFN_PAD, LANES), F32)
        _fill_head(hg_ref, ug_ref, FFN_PAD)
        _fill_head(hv_ref, uv_ref, FFN_PAD)
        dg_ref[S:, :] = zeros
        dv_ref[S:, :] = zeros
        dwg_acc[...] = jnp.zeros_like(dwg_acc)
        dwv_acc[...] = jnp.zeros_like(dwv_acc)
        dbg_acc[...] = jnp.zeros_like(dbg_acc)
        dbv_acc[...] = jnp.zeros_like(dbv_acc)

        def first(t0, xg_ref, xv_ref, x_t0, pad):
            rows = pl.ds(t0, TIME_BLOCK)
            cg = _conv_block(xg_ref, x_t0, wg_ref, FFN_CONV_WIDTH, pad) + bg_ref[...]
            cv = _conv_block(xv_ref, x_t0, wv_ref, FFN_CONV_WIDTH, pad) + bv_ref[...]
            dfb = df_ref[rows, :]
            gelu, gelu_grad = _gelu_and_grad(cg)
            dcg = dfb * cv * gelu_grad
            dcv = dfb * gelu
            dg_ref[rows, :] = dcg
            dv_ref[rows, :] = dcv
            _conv_weight_grad(xg_ref, x_t0, dcg, dwg_acc, FFN_CONV_WIDTH, pad)
            _conv_weight_grad(xv_ref, x_t0, dcv, dwv_acc, FFN_CONV_WIDTH, pad)
            dbg_acc[...] += jnp.sum(dcg, axis=0, keepdims=True)
            dbv_acc[...] += jnp.sum(dcv, axis=0, keepdims=True)

        def second(t0):
            rows = pl.ds(t0, TIME_BLOCK)
            du_ref[0, rows, :] = _conv_transpose_block(dg_ref, t0, wg_ref, FFN_CONV_WIDTH).astype(BF16)
            du_ref[1, rows, :] = _conv_transpose_block(dv_ref, t0, wv_ref, FFN_CONV_WIDTH).astype(BF16)

        first(0, hg_ref, hv_ref, 0, FFN_PAD)
        _time_loop(S, lambda t0: first(t0, ug_ref, uv_ref, t0, 0), skip_first=1)
        _time_loop(S, second)
        dwg_ref[...] = dwg_acc[...]
        dwv_ref[...] = dwv_acc[...]
        dbg_ref[...] = dbg_acc[...]
        dbv_ref[...] = dbv_acc[...]

    seq = lambda off: pl.BlockSpec((S, LANES), lambda i: (0, off + i))
    wsp = lambda off: pl.BlockSpec((FFN_CONV_WIDTH, LANES), lambda i: (0, off + i))
    bsp = lambda off: pl.BlockSpec((1, LANES), lambda i: (0, off + i))
    return pl.pallas_call(
        body, name="ffn_conv_geglu_bwd", grid=(nt,),
        in_specs=[seq(0), seq(nt), wsp(0), wsp(nt), bsp(0), bsp(nt), seq(0)],
        out_specs=[pl.BlockSpec((2, S, LANES), lambda i: (0, 0, i)),
                   pl.BlockSpec((SUBLANES, LANES), lambda i: (0, i)), pl.BlockSpec((SUBLANES, LANES), lambda i: (0, i)),
                   bsp(0), bsp(0)],
        out_shape=[jax.ShapeDtypeStruct((2, S, C), BF16)] + [jax.ShapeDtypeStruct((SUBLANES, C), F32)] * 2
        + [jax.ShapeDtypeStruct((1, C), F32)] * 2,
        scratch_shapes=[pltpu.VMEM((FFN_PAD + TIME_BLOCK, LANES), F32)] * 2 + [pltpu.VMEM((S + FFN_PAD, LANES), F32)] * 2
        + [pltpu.VMEM((SUBLANES, LANES), F32)] * 2
        + [pltpu.VMEM((1, LANES), F32)] * 2,
        compiler_params=_params(("parallel",)),
    )(u, u, w, w, b, b, df)


def _adamw(w_ref, g_ref, m_ref, v_ref, d_ref, mo_ref, vo_ref):
    gv = g_ref[...]
    mn = ADAM_B1 * m_ref[...] + (1.0 - ADAM_B1) * gv
    vn = ADAM_B2 * v_ref[...] + (1.0 - ADAM_B2) * (gv * gv)
    mo_ref[...] = mn
    vo_ref[...] = vn
    m_hat = mn * (1.0 / (1.0 - ADAM_B1 ** ADAM_STEP))
    v_hat = vn * (1.0 / (1.0 - ADAM_B2 ** ADAM_STEP))
    d_ref[...] = -ADAM_LR * (m_hat / (jnp.sqrt(v_hat) + ADAM_EPS) + ADAM_WD * w_ref[...])


def _adamw_call(w, g, m, v, name):
    R, C = w.shape
    tr = _row_tile(R, C)

    def body(w_ref, g_ref, m_ref, v_ref, go_ref, d_ref, mo_ref, vo_ref):
        go_ref[...] = g_ref[...]
        _adamw(w_ref, g_ref, m_ref, v_ref, d_ref, mo_ref, vo_ref)

    spec = pl.BlockSpec((tr, C), lambda i: (i, 0))
    return pl.pallas_call(
        body, name=name, grid=(R // tr,),
        in_specs=[spec] * 4, out_specs=[spec] * 4,
        out_shape=[jax.ShapeDtypeStruct((R, C), F32)] * 4,
        compiler_params=_params(("parallel",)),
    )(w, g, m, v)


def _adamw_small_call(ws, gs, ms, vs):
    n = len(ws)

    def body(*refs):
        w_refs, g_refs, m_refs, v_refs, d_refs, mo_refs, vo_refs = (refs[i * n:(i + 1) * n] for i in range(7))
        for i in range(n):
            _adamw(w_refs[i], g_refs[i], m_refs[i], v_refs[i], d_refs[i], mo_refs[i], vo_refs[i])

    whole = pl.BlockSpec(memory_space=pltpu.VMEM)
    outs = pl.pallas_call(
        body, name="adamw_small",
        in_specs=[whole] * (4 * n), out_specs=[whole] * (3 * n),
        out_shape=[jax.ShapeDtypeStruct(w.shape, F32) for w in ws] * 3,
    )(*ws, *gs, *ms, *vs)
    return outs[:n], outs[n:2 * n], outs[2 * n:]


def _position():
    return lax.axis_index("x"), lax.axis_index("y"), lax.axis_index("c")


def _chip_peers(x, y):
    return [(x, 1 - y), (1 - x, y), (1 - x, 1 - y)]


def _half_rows(ref, core, rows):
    h = rows // 2
    start = pl.multiple_of(core * h, PACKED_ROWS)
    return ref.at[pl.ds(start, h), :] if len(ref.shape) == 2 else ref.at[:, pl.ds(start, h), :]


def _shard_half(ref, shard, core, rows):
    h = rows // 2
    return ref.at[shard, pl.ds(pl.multiple_of(core * h, PACKED_ROWS), h), :]


ANY = pl.BlockSpec(memory_space=pl.ANY)


def _first_hop_copies(srcs, lands):
    x, y, c = _position()
    chip = 2 * x + y
    targets = [(px, py, c) for px, py in _chip_peers(x, y)] + [(x, y, 1 - c)]
    rows = srcs[0].shape[0]
    out = []
    for i, (s, l) in enumerate(zip(srcs, lands)):
        for k, dev in enumerate(targets):
            if i == 0 and k < 3:
                out.append((_half_rows(s, c, rows), _shard_half(l, chip, c, rows), dev, k))
            else:
                out.append((s, l.at[chip], dev, len(targets) * i + k))
    return out


def _second_hop_copies(srcs, lands):
    x, y, c = _position()
    rows = lands[0].shape[1]
    out = []
    for k, (px, py) in enumerate(_chip_peers(x, y)):
        half = _shard_half(lands[0], 2 * px + py, c, rows)
        out.append((half, half, (x, y, 1 - c), k))
    return out


HBM_SPEC = pl.BlockSpec(memory_space=pltpu.HBM)
SEM_SPEC = pl.BlockSpec(memory_space=pltpu.SEMAPHORE)
DATAFLOW = pltpu.SideEffectType.DATAFLOW_SIDE_EFFECTING


def _in_hbm(a):
    return pltpu.with_memory_space_constraint(a, pltpu.HBM)


SIBLING_HANDSHAKE_IDS = dict(grad_exchange_start_w_up=1, grad_assemble_start_others=2, grad_assemble_start_w_in=3)


def _split_start(name, groups, after, carry=None, sibling_only=False):
    spans, arrays = [], []
    for srcs, lands, _, _ in groups:
        spans.append((len(arrays), len(srcs), len(lands)))
        arrays += list(srcs) + list(lands)
    if carry is not None:
        arrays.append(carry)
    na, ng = len(arrays), len(groups)

    def body(*refs):
        sems, token = refs[na + 1:na + 1 + 2 * ng], refs[-1]
        if sibling_only:
            x, y, c = _position()
            barrier = pltpu.get_barrier_semaphore()
            pl.semaphore_signal(barrier, inc=1, device_id=(x, y, 1 - c), device_id_type=MESH)
            pl.semaphore_wait(barrier, 1)
        for g, (_, _, _, copies) in enumerate(groups):
            off, ns, nl = spans[g]
            for src, dst, dev, idx in copies(refs[off:off + ns], refs[off + ns:off + ns + nl]):
                pltpu.make_async_remote_copy(src_ref=src, dst_ref=dst, send_sem=sems[2 * g].at[idx], recv_sem=sems[2 * g + 1].at[idx],
                                             device_id=dev, device_id_type=MESH).start()
        token[...] = jnp.zeros_like(token)

    outs = pl.pallas_call(
        body, name=name,
        in_specs=[HBM_SPEC] * na + [ANY],
        out_specs=[SEM_SPEC] * (2 * ng) + [HBM_SPEC] * na + [pl.BlockSpec(memory_space=pltpu.VMEM)],
        out_shape=[pltpu.SemaphoreType.DMA((n_sems,)) for _, _, n_sems, _ in groups for _ in range(2)]
        + [pltpu.HBM(a.shape, a.dtype) for a in arrays] + [jax.ShapeDtypeStruct((SUBLANES, LANES), F32)],
        input_output_aliases={i: 2 * ng + i for i in range(na)},
        compiler_params=pltpu.CompilerParams(has_side_effects=DATAFLOW,
                                             collective_id=SIBLING_HANDSHAKE_IDS[name] if sibling_only else None),
    )(*[_in_hbm(a) for a in arrays], after)
    started = []
    for g, (off, ns, nl) in enumerate(spans):
        thru = outs[2 * ng + off:2 * ng + off + ns + nl]
        started.append(dict(send=outs[2 * g], recv=outs[2 * g + 1], srcs=list(thru[:ns]), lands=list(thru[ns:]),
                            tile=outs[-1], token=outs[-1][0, 0], carry=None if carry is None else outs[2 * ng + na - 1]))
    return started


def _split_wait(name, started, copies, after):
    n, m = len(started["srcs"]), len(started["lands"])
    after = list(after) if isinstance(after, (list, tuple)) else [after]

    def body(*refs):
        src_refs, land_refs = refs[:n], refs[n:n + m]
        send_sem, recv_sem = refs[n + m], refs[n + m + 1]
        for src, dst, dev, idx in copies(src_refs, land_refs):
            cp = pltpu.make_async_remote_copy(src_ref=src, dst_ref=dst, send_sem=send_sem.at[idx], recv_sem=recv_sem.at[idx],
                                              device_id=dev, device_id_type=MESH)
            cp.wait_send()
            cp.wait_recv()

    arrays = started["srcs"] + started["lands"]
    outs = pl.pallas_call(
        body, name=name,
        in_specs=[HBM_SPEC] * (n + m) + [SEM_SPEC, SEM_SPEC] + [ANY] * len(after),
        out_specs=[HBM_SPEC] * (n + m),
        out_shape=[pltpu.HBM(a.shape, a.dtype) for a in arrays],
        input_output_aliases={i: i for i in range(n + m)},
        compiler_params=pltpu.CompilerParams(has_side_effects=DATAFLOW),
    )(*arrays, started["send"], started["recv"], *after)
    return list(outs)


def _gather_copies(srcs, lands):
    x, y, c = _position()
    chip = 2 * x + y
    targets = [(px, py, c) for px, py in _chip_peers(x, y)] + [(x, y, 1 - c)]
    return [(s, l.at[chip], dev, len(targets) * i + k) for i, (s, l) in enumerate(zip(srcs, lands)) for k, dev in enumerate(targets)]


def _sibling_copies(srcs, lands):
    x, y, c = _position()
    return [(_half_rows(srcs[0], 1 - c, srcs[0].shape[1]), lands[0], (x, y, 1 - c), 0)]


def _sibling_whole_copies(srcs, lands):
    x, y, c = _position()
    return [(srcs[0], lands[0], (x, y, 1 - c), 0)]


def _exchange_copies(srcs, lands):
    x, y, c = _position()
    return [(srcs[0].at[2 * px + py], lands[0].at[k], (px, py, c), k) for k, (px, py) in enumerate(_chip_peers(x, y))]


def _pair_sum_call(grad, recv, chip_core, name):
    _, h, B = recv.shape
    tr = _row_tile(h, B)

    def body(cc_ref, g_ref, r_ref, o_ref, ob_ref):
        s = g_ref[...] + r_ref[...]
        ob_ref[...] = s.astype(BF16)

        @pl.when(pl.program_id(1) == cc_ref[0])
        def _():
            o_ref[...] = s

    g_spec = pl.BlockSpec((None, tr, B), lambda i, q, cc_ref: (q, cc_ref[1] * (h // tr) + i, 0))
    spec = pl.BlockSpec((None, tr, B), lambda i, q, cc_ref: (q, i, 0))
    own_spec = pl.BlockSpec((tr, B), lambda i, q, cc_ref: (i, 0))
    return pl.pallas_call(
        body, name=name,
        grid_spec=pltpu.PrefetchScalarGridSpec(num_scalar_prefetch=1, grid=(h // tr, N_CHIPS), in_specs=[g_spec, spec],
                                               out_specs=[own_spec, spec]),
        out_shape=[jax.ShapeDtypeStruct((h, B), F32), jax.ShapeDtypeStruct(recv.shape, BF16)],
        compiler_params=_params(("parallel", "arbitrary")),
    )(chip_core, grad, recv)


def _chip_sum_call(partial, recv, chip_core, name):
    _, h, B = recv.shape
    tr = _row_tile(h, B)

    def body(cc_ref, p_ref, r_ref, o_ref):
        o_ref[...] = ((p_ref[...] + r_ref[0].astype(F32)) + r_ref[1].astype(F32)) + r_ref[2].astype(F32)

    return pl.pallas_call(
        body, name=name,
        grid_spec=pltpu.PrefetchScalarGridSpec(
            num_scalar_prefetch=1, grid=(h // tr,),
            in_specs=[pl.BlockSpec((tr, B), lambda i, cc_ref: (i, 0)),
                      pl.BlockSpec((3, tr, B), lambda i, cc_ref: (0, i, 0))],
            out_specs=pl.BlockSpec((tr, B), lambda i, cc_ref: (cc_ref[1] * (h // tr) + i, 0))),
        out_shape=jax.ShapeDtypeStruct((2 * h, B), F32),
        compiler_params=_params(("parallel",)),
    )(chip_core, partial, recv)


def _assemble_copies(srcs, lands):
    x, y, c = _position()
    out = []
    for i, land in enumerate(lands):
        half = _half_rows(land, c, land.shape[0])
        out.append((half, half, (x, y, 1 - c), i))
    return out


N_DEVICES = 8


def _allsum_copies(srcs, lands):
    x, y, c = _position()
    me = 4 * x + 2 * y + c
    out = []
    for k in range(1, N_DEVICES):
        peer = (1 - x if k & 4 else x, 1 - y if k & 2 else y, 1 - c if k & 1 else c)
        out.append((srcs[0], lands[0].at[me], peer, k - 1))
    return out


def _ordered_sum_call(mine, landed, me_chip, shapes, sharded_cols):
    rows = mine.shape[0]
    outs = [(s[0], n) if n else s for s, n in zip(shapes, sharded_cols)]

    def body(mc_ref, x_ref, l_ref, *refs):
        acc_ref = refs[-1]
        acc = jnp.where(mc_ref[0] == 0, x_ref[...], l_ref[0])
        for d in range(1, N_DEVICES):
            acc = acc + jnp.where(mc_ref[0] == d, x_ref[...], l_ref[d])
        acc_ref[...] = acc
        first = 0
        for o_ref, (r, c), n in zip(refs[:-1], shapes, sharded_cols):
            per_row = c // LANES

            def unpack(chip, o_ref=o_ref, r=r, n=n, per_row=per_row, first=first):
                for i in range(r):
                    for j in range((n or per_row * LANES) // LANES):
                        src = first + i * per_row + chip * ((n or 0) // LANES) + j
                        o_ref[i:i + 1, j * LANES:(j + 1) * LANES] = acc_ref[src:src + 1, :]

            if n:
                for q in range(N_CHIPS):
                    pl.when(mc_ref[1] == q)(functools.partial(unpack, q))
            else:
                unpack(0)
            first += r * per_row

    results = pl.pallas_call(
        body, name="small_grad_sum",
        in_specs=[pl.BlockSpec(memory_space=pltpu.SMEM), pl.BlockSpec(memory_space=pltpu.VMEM), pl.BlockSpec(memory_space=pltpu.VMEM)],
        out_specs=[pl.BlockSpec(memory_space=pltpu.VMEM)] * len(outs),
        out_shape=[jax.ShapeDtypeStruct(s, F32) for s in outs],
        scratch_shapes=[pltpu.VMEM((rows, LANES), F32)],
    )(me_chip, mine, landed)
    return results


def _pack(arrays):
    flat = jnp.concatenate([a.reshape(-1).astype(F32) for a in arrays])
    rows = -(-flat.shape[0] // LANES)
    rows = -(-rows // SUBLANES) * SUBLANES
    flat = jnp.pad(flat, (0, rows * LANES - flat.shape[0]))
    return flat.reshape(rows, LANES)


def _local_step(xs, target, P, late_weights, on_grad):
    S, D = xs.shape
    qkv_width = 3 * N_HEADS * HEAD_DIM
    glu_col0, gate_col0 = qkv_width, qkv_width + 2 * D
    shard_major = lambda g: g.reshape(N_CHIPS, g.shape[0] // N_CHIPS, g.shape[1])

    h1 = _rms_fwd_call(xs, P["norm_mix_pre"])
    buckets = _bucket_tables()
    bias = _bias_table_call(P["rel_bias"] + 0.0 * h1[0, 0].astype(F32), buckets)
    P = dict(P, **late_weights("in", bias))
    proj = _matmul(h1, P["w_in"], "nn", "proj_in")
    dilated = []
    for g in range(1, N_GROUPS):
        dilated += _attn_fwd_call(proj, bias, g)
    a, a_bf, lse = _attn_fwd_call(proj, bias, 0, merge_with=dilated)
    P = dict(P, **late_weights("mix", a_bf))
    y_a = _matmul(a_bf, P["w_attn_out"], "nn", "attn_out")
    c1 = _conv_fwd_call(proj, glu_col0, P["conv_dw_w"], P["conv_dw_b"])
    cact = _ln_silu_call(c1, P["conv_ln_g"], P["conv_ln_b"])
    y_c = _matmul(cact, P["conv_pw_w"], "nn", "conv_pw")
    mixed = _mix_call(proj, gate_col0, P["b_gate"], y_a, y_c)
    out = _matmul(mixed, P["w_out"], "nn", "mix_out")
    x1, h2 = _res1_call(xs, out, P["norm_mix_post"], P["norm_ffn_pre"])
    P = dict(P, **late_weights("up", h2))
    u = _matmul(h2, P["w_up"], "nn", "ffn_up")
    f = _ffn_fwd_call(u, P["ffn_conv_w"], P["ffn_conv_b"])
    P = dict(P, **late_weights("down", f))
    loss_tile, dx2, dyff, dg_ffn_post = _ffn_down_loss_call(f, P["w_down"], x1, P["norm_ffn_post"], target)

    G = {}
    G["norm_ffn_post"] = dg_ffn_post
    on_grad("w_down", shard_major(_matmul(f, dyff, "tn", "ffn_down_dw")))
    df = _matmul(dyff, P["w_down"], "nt", "ffn_down_dx")
    du, dwg, dwv, dbg, dbv = _ffn_bwd_call(u, P["ffn_conv_w"], P["ffn_conv_b"], df)
    G["ffn_conv_w"] = jnp.concatenate([dwg[:FFN_CONV_WIDTH], dwv[:FFN_CONV_WIDTH]], axis=1)
    G["ffn_conv_b"] = jnp.concatenate([dbg, dbv], axis=1)
    du = on_grad("w_up", functools.partial(_grad_half_matmul, h2, "ffn_up_dw"), carry=du)
    dh2 = _matmul(du, P["w_up"], "nt", "ffn_up_dx")
    dx1, dout, G["norm_ffn_pre"], G["norm_mix_post"] = _mid_bwd_call(x1, P["norm_ffn_pre"], dh2, dx2, out, P["norm_mix_post"])
    on_grad("w_out", shard_major(_matmul(mixed, dout, "tn", "mix_out_dw")))
    dya, dyc, dproj, dba, dbc = _mix_bwd_call(dout, P["w_out"], proj, gate_col0, P["b_gate"], y_a, y_c)
    G["b_gate"] = jnp.concatenate([dba, dbc], axis=1)
    on_grad("w_attn_out", _matmul(a_bf, dya, "tn", "attn_out_dw", out_shards=True))
    dyc = on_grad("conv_pw_w", shard_major(_matmul(cact, dyc, "tn", "conv_pw_dw")), carry=dyc)
    da = _matmul(dya, P["w_attn_out"], "nt", "attn_out_dx")
    dc1, G["conv_ln_g"], G["conv_ln_b"] = _ln_silu_bwd_call(c1, P["conv_ln_g"], P["conv_ln_b"], dyc, P["conv_pw_w"])
    dproj, dw_dw, G["conv_dw_b"] = _conv_bwd_call(proj, glu_col0, P["conv_dw_w"], dc1, dproj)
    G["conv_dw_w"] = dw_dw[:CONV_WIDTH]
    dbs = []
    for g in range(N_GROUPS):
        dproj, db = _attn_bwd_call(proj, bias, a, da, lse, g, dproj)
        dbs.append(db)
    G["rel_bias"] = _bias_grad_call(jnp.concatenate(dbs, axis=0), buckets)
    dproj = on_grad("w_in", functools.partial(_grad_half_matmul, h1, "proj_in_dw"), carry=dproj)
    dh1 = _matmul(dproj, P["w_in"], "nt", "proj_in_dx")
    dh1 = on_grad(None, None, carry=dh1)
    grad_x, G["norm_mix_pre"] = _in_bwd_call(xs, P["norm_mix_pre"], dh1, dx1)
    return loss_tile, grad_x, G


def kernel(x, w_in, b_gate, rel_bias, w_attn_out, conv_dw_w, conv_dw_b, conv_ln_g, conv_ln_b, conv_pw_w, w_out, norm_mix_pre, norm_mix_post, norm_ffn_pre, norm_ffn_post, w_up, ffn_conv_w, ffn_conv_b, w_down, loss_target, m_w_in, m_b_gate, m_rel_bias, m_w_attn_out, m_conv_dw_w, m_conv_dw_b, m_conv_ln_g, m_conv_ln_b, m_conv_pw_w, m_w_out, m_norm_mix_pre, m_norm_mix_post, m_norm_ffn_pre, m_norm_ffn_post, m_w_up, m_ffn_conv_w, m_ffn_conv_b, m_w_down, v_w_in, v_b_gate, v_rel_bias, v_w_attn_out, v_conv_dw_w, v_conv_dw_b, v_conv_ln_g, v_conv_ln_b, v_conv_pw_w, v_w_out, v_norm_mix_pre, v_norm_mix_post, v_norm_ffn_pre, v_norm_ffn_post, v_w_up, v_ffn_conv_w, v_ffn_conv_b, v_w_down):
    weights = dict(w_in=w_in, b_gate=b_gate, rel_bias=rel_bias, w_attn_out=w_attn_out, conv_dw_w=conv_dw_w, conv_dw_b=conv_dw_b,
                   conv_ln_g=conv_ln_g, conv_ln_b=conv_ln_b, conv_pw_w=conv_pw_w, w_out=w_out, norm_mix_pre=norm_mix_pre,
                   norm_mix_post=norm_mix_post, norm_ffn_pre=norm_ffn_pre, norm_ffn_post=norm_ffn_post, w_up=w_up,
                   ffn_conv_w=ffn_conv_w, ffn_conv_b=ffn_conv_b, w_down=w_down)
    m_in = dict(w_in=m_w_in, b_gate=m_b_gate, rel_bias=m_rel_bias, w_attn_out=m_w_attn_out, conv_dw_w=m_conv_dw_w,
                conv_dw_b=m_conv_dw_b, conv_ln_g=m_conv_ln_g, conv_ln_b=m_conv_ln_b, conv_pw_w=m_conv_pw_w, w_out=m_w_out,
                norm_mix_pre=m_norm_mix_pre, norm_mix_post=m_norm_mix_post, norm_ffn_pre=m_norm_ffn_pre,
                norm_ffn_post=m_norm_ffn_post, w_up=m_w_up, ffn_conv_w=m_ffn_conv_w, ffn_conv_b=m_ffn_conv_b, w_down=m_w_down)
    v_in = dict(w_in=v_w_in, b_gate=v_b_gate, rel_bias=v_rel_bias, w_attn_out=v_w_attn_out, conv_dw_w=v_conv_dw_w,
                conv_dw_b=v_conv_dw_b, conv_ln_g=v_conv_ln_g, conv_ln_b=v_conv_ln_b, conv_pw_w=v_conv_pw_w, w_out=v_w_out,
                norm_mix_pre=v_norm_mix_pre, norm_mix_post=v_norm_mix_post, norm_ffn_pre=v_norm_ffn_pre,
                norm_ffn_post=v_norm_ffn_post, w_up=v_w_up, ffn_conv_w=v_ffn_conv_w, ffn_conv_b=v_ffn_conv_b, w_down=v_w_down)
    names = list(weights)
    xi, yi, ci = _position()
    chip = 2 * xi + yi
    core_arr = jnp.reshape(ci, (1,)).astype(jnp.int32)

    xs = x[0]
    target = loss_target[0]
    S, D = xs.shape

    big = ["w_in", "w_attn_out", "conv_pw_w", "w_out", "w_up", "w_down"]
    row_sharded = ("conv_pw_w", "w_out", "w_down")
    natural = lambda k, g: g.reshape(-1, g.shape[2]) if k in row_sharded else g
    first_srcs = [w_in[0].astype(BF16), conv_dw_w[0], ffn_conv_w[0]]
    first_lands = [lax.empty((N_CHIPS,) + s.shape, s.dtype) for s in first_srcs]
    (first_hop,) = _split_start("gather_in_start", [(first_srcs, first_lands, 4 * len(first_srcs), _first_hop_copies)], core_arr)
    launched = first_hop["token"]
    late_sets = dict(mix=["w_attn_out", "conv_pw_w", "w_out"], up=["w_up"], down=["w_down"])
    late_groups = []
    for keys in late_sets.values():
        srcs = [(weights[k][0] + launched).astype(BF16) for k in keys]
        late_groups.append((srcs, [lax.empty((N_CHIPS,) + s.shape, BF16) for s in srcs], 4 * len(keys), _gather_copies))
    started = {}

    def late_weights(tag, after):
        if tag == "in":
            casts = [s for srcs, _, _, _ in late_groups for s in srcs]
            w_in_halves, dw4, fc4 = _split_wait("gather_in_wait", first_hop, _first_hop_copies, [after] + casts)[len(first_srcs):]
            second_hop, *late = _split_start("gather_in_pass_start", [([], [w_in_halves], 3, _second_hop_copies)] + late_groups, dw4)
            started.update(zip(late_sets, late))
            (w_in_full,) = _split_wait("gather_in_pass_wait", second_hop, _second_hop_copies, second_hop["tile"])
            return dict(w_in=w_in_full, conv_dw_w=jnp.concatenate(list(dw4), axis=1), ffn_conv_w=jnp.concatenate(list(fc4), axis=1))
        landed = _split_wait(f"gather_{tag}_wait", started[tag], _gather_copies, after)[len(late_sets[tag]):]
        return {k: natural(k, g) for k, g in zip(late_sets[tag], landed)}

    chip_core = jnp.stack([chip, ci]).astype(jnp.int32)
    exchanging, pending, second_half = {}, {}, {}

    held = []

    def launch(tag, after, carry=None):
        keys, groups, partial = [], [], {}
        for k in list(exchanging):
            st, copies = exchanging.pop(k)
            gk, r1 = _split_wait(f"sibling_exchange_wait_{k}", st, copies, after)
            if k in second_half:
                partial[k], s16 = second_half.pop(k)(init=r1)
            else:
                partial[k], s16 = _pair_sum_call(gk, r1, chip_core, f"pair_sum_{k}")
            keys.append(k)
            groups.append(([s16], [lax.empty((3,) + s16.shape[1:], BF16)], 3, _exchange_copies))
        fresh = [(k, copies) for k, _, copies in held]
        for _, g3, copies in held:
            rows = g3.shape[1] // 2 if copies is _sibling_copies else g3.shape[1]
            groups.append(([g3], [lax.empty((N_CHIPS, rows, g3.shape[2]), F32)], 1, copies))
        held.clear()
        begun = _split_start(f"grad_exchange_start_{tag}", groups, core_arr, carry, sibling_only=not keys)
        for k, st in zip(keys, begun):
            pending[k] = (partial[k], st)
        for (k, copies), st in zip(fresh, begun[len(keys):]):
            exchanging[k] = (st, copies)
        return begun[0]["carry"]

    others = [k for k in big if k != "w_in"]
    assembling = {}

    def on_grad(k, g, carry=None):
        if k is None:
            carried = launch("last", carry[:SUBLANES, :LANES], carry)
            assembling["others"] = assemble_start(others, carried, "others", carried)
            return assembling["others"]["carry"]
        if callable(g):
            theirs = g(jnp.stack([chip, 1 - ci]).astype(jnp.int32), carry)
            held.append((k, theirs, _sibling_whole_copies))
            carried = launch(k, theirs[0, :SUBLANES, :LANES], carry)
            second_half[k] = functools.partial(g, chip_core, carried)
            return carried
        held.append((k, g, _sibling_copies))
        if k in ("w_down", "w_out", "w_attn_out"):
            return carry
        return launch(k, g[0, :SUBLANES, :LANES], carry)

    def assemble_start(keys, after, tag, carry=None):
        halves = []
        for k in keys:
            s32, st = pending[k]
            recv2 = _split_wait(f"chip_exchange_wait_{k}", st, _exchange_copies, after)[1]
            halves.append(_chip_sum_call(s32, recv2, chip_core, f"chip_sum_{k}"))
        (st,) = _split_start(f"grad_assemble_start_{tag}", [([], halves, len(halves), _assemble_copies)], core_arr, carry,
                             sibling_only=True)
        return st

    def assemble_wait(keys, st, after, tag):
        return dict(zip(keys, _split_wait(f"grad_assemble_wait_{tag}", st, _assemble_copies, after)))

    P = dict(b_gate=b_gate, rel_bias=rel_bias, conv_dw_b=conv_dw_b, conv_ln_g=conv_ln_g, conv_ln_b=conv_ln_b,
             norm_mix_pre=norm_mix_pre + launched, norm_mix_post=norm_mix_post, norm_ffn_pre=norm_ffn_pre,
             norm_ffn_post=norm_ffn_post, ffn_conv_b=ffn_conv_b)
    loss_tile, grad_x, G = _local_step(xs, target, P, late_weights, on_grad)

    small = [k for k in names if k not in big]
    packed = _pack([loss_tile[:1]] + [G[k] for k in small])
    (allsum,) = _split_start("small_grad_allsum_start",
                             [([packed], [jnp.zeros((N_DEVICES,) + packed.shape, F32)], N_DEVICES - 1, _allsum_copies)], core_arr)

    reduced, grads, deltas, new_m, new_v = {}, {}, {}, {}, {}

    def update(keys):
        for k in keys:
            gk, d, mn, vn = _adamw_call(weights[k][0], reduced[k], m_in[k][0], v_in[k][0], f"adamw_{k}")
            grads[k], deltas[k], new_m[k], new_v[k] = gk[None], d[None], mn[None], vn[None]

    reduced.update(assemble_wait(others, assembling["others"], [allsum["tile"], grad_x], "others"))
    update(others)
    assembling["w_in"] = assemble_start(["w_in"], [deltas[k] for k in others], "w_in")

    me_chip = jnp.stack([4 * xi + 2 * yi + ci, chip]).astype(jnp.int32)
    mine, landed = _split_wait("small_grad_allsum_wait", allsum, _allsum_copies, assembling["w_in"]["tile"])
    piece_shapes = [(1, LANES)] + [(G[k].size // LANES, LANES) if k == "rel_bias" else G[k].shape for k in small]
    piece_cols = [0] + [weights[k].shape[2] if k in ("conv_dw_w", "ffn_conv_w") else 0 for k in small]
    loss_row, *summed = _ordered_sum_call(mine, landed, me_chip, piece_shapes, piece_cols)
    loss = loss_row[0, 0]
    for k, gsum in zip(small, summed):
        grads[k] = gsum.reshape(weights[k].shape)
    ds, mns, vns = _adamw_small_call([weights[k] for k in small], [grads[k] for k in small],
                                     [m_in[k] for k in small], [v_in[k] for k in small])
    deltas.update(zip(small, ds))
    new_m.update(zip(small, mns))
    new_v.update(zip(small, vns))
    reduced.update(assemble_wait(["w_in"], assembling["w_in"], list(ds), "w_in"))
    update(["w_in"])

    return (loss, grad_x[None], *[grads[k] for k in names], *[deltas[k] for k in names],
            *[new_m[k] for k in names], *[new_v[k] for k in names])
```
